```python
import jax, jax.numpy as jnp
from jax import lax
import numpy as np

D_MODEL = 1024
BATCH = 8
SEQ = 2048
DEPTH = 1

HEAD_DIM = 64
N_SLOT_HEADS = 8
DILATED_PATTERNS = ((128, 1), (512, 4), (2048, 16))
N_GROUPS = len(DILATED_PATTERNS)
ATTN_HEADS = N_GROUPS * N_SLOT_HEADS
ATTN_QKV_WIDTH = ATTN_HEADS * HEAD_DIM
ATTN_OUT_WIDTH = N_SLOT_HEADS * HEAD_DIM
BLK = max(w // (2 * d) for (w, d) in DILATED_PATTERNS)
ROPE_THETA = 500000.0
ROT_DIM = HEAD_DIM // 4
CONV_CH = D_MODEL // 2
CONV_WIDTH = 31
N_BRANCH = 2
D_FF = -(-8 * D_MODEL // 768) * 256
IN_WIDTH = 3 * ATTN_QKV_WIDTH + 2 * CONV_CH + N_BRANCH * D_MODEL
EPS = 1e-6
NEG_INF = -1e30

kernel_name = 'hybrid_dilated_attn_conformer_conv_block'


def rmsnorm(t, w):
    tf = t.astype(jnp.float32)
    y = tf * lax.rsqrt(jnp.mean(tf * tf, axis=-1, keepdims=True) + EPS)
    return (y * w.astype(jnp.float32)).astype(t.dtype)


def layernorm(t, w, b):
    tf = t.astype(jnp.float32)
    mu = jnp.mean(tf, axis=-1, keepdims=True)
    var = jnp.mean(jnp.square(tf - mu), axis=-1, keepdims=True)
    y = (tf - mu) * lax.rsqrt(var + EPS)
    return (y * w.astype(jnp.float32) + b.astype(jnp.float32)).astype(t.dtype)


def partial_rope(t, cos, sin):
    tf = t.astype(jnp.float32)
    half = ROT_DIM // 2
    t1, t2, rest = tf[..., :half], tf[..., half:ROT_DIM], tf[..., ROT_DIM:]
    rot = jnp.concatenate([t1 * cos - t2 * sin, t2 * cos + t1 * sin, rest], axis=-1)
    return rot.astype(t.dtype)


def dilated_window_attention(q, k, v, dilation, half_span):
    B, S, H, Dh = q.shape
    L = S // dilation
    nb = -(-L // BLK)
    Lp = nb * BLK

    def residue_major(t):
        return t.reshape(B, L, dilation, H, Dh).transpose(0, 2, 3, 1, 4)

    qr, kr, vr = residue_major(q), residue_major(k), residue_major(v)
    qb = jnp.pad(qr, [(0, 0)] * 3 + [(0, Lp - L), (0, 0)]).reshape(B, dilation, H, nb, BLK, Dh)

    def banded(t):
        tb = jnp.pad(t, [(0, 0)] * 3 + [(BLK, Lp - L + BLK), (0, 0)])
        tb = tb.reshape(B, dilation, H, nb + 2, BLK, Dh)
        return jnp.concatenate([tb[:, :, :, j:j + nb] for j in range(3)], axis=4)

    kb, vb = banded(kr), banded(vr)
    qpos = jnp.arange(nb)[:, None] * BLK + jnp.arange(BLK)[None, :]
    kpos = jnp.arange(nb)[:, None] * BLK - BLK + jnp.arange(3 * BLK)[None, :]
    dist = jnp.abs(qpos[:, :, None] - kpos[:, None, :])
    valid = (dist <= half_span) & (kpos[:, None, :] >= 0) & (kpos[:, None, :] < L)

    s = jnp.einsum('bdhnqc,bdhnkc->bdhnqk', qb.astype(jnp.float32), kb.astype(jnp.float32))
    s = jnp.where(valid, s * (HEAD_DIM ** -0.5), NEG_INF)
    m = jnp.max(s, axis=-1, keepdims=True)
    p = jnp.exp(s - m)
    den = jnp.sum(p, axis=-1, keepdims=True)
    o = jnp.einsum('bdhnqk,bdhnkc->bdhnqc', p, vb.astype(jnp.float32)) / den
    lse = (m + jnp.log(den))[..., 0]
    o = o.reshape(B, dilation, H, Lp, Dh)[:, :, :, :L].transpose(0, 3, 1, 2, 4).reshape(B, S, H, Dh)
    lse = lse.reshape(B, dilation, H, Lp)[..., :L].transpose(0, 3, 1, 2).reshape(B, S, H)
    return o, lse


def depthwise_conv(u, w, b):
    pad = (CONV_WIDTH - 1) // 2
    y = lax.conv_general_dilated(
        u, w[:, None, :].astype(u.dtype), window_strides=(1,), padding=[(pad, pad)],
        dimension_numbers=('NWC', 'WIO', 'NWC'), feature_group_count=CONV_CH)
    return y + b.astype(u.dtype)


def _fwd_setup_inputs(seed: int = 0) -> dict:
    key = jax.random.key(seed)
    ks = jax.random.split(key, 20)
    f32 = jnp.float32

    def nrm(k, shape, scale):
        return jax.random.normal(k, shape, f32) * scale

    x = jax.random.normal(ks[0], (BATCH, SEQ, D_MODEL), f32)
    offsets = jax.random.randint(ks[1], (BATCH, 1), 0, 4096, dtype=jnp.int32)
    positions = (offsets + jnp.arange(SEQ, dtype=jnp.int32)[None, :]).astype(jnp.int32)
    return {
        'x': x,
        'positions': positions,
        'norm1_w': 1.0 + nrm(ks[2], (DEPTH, D_MODEL), 0.02),
        'w_in': nrm(ks[3], (DEPTH, D_MODEL, IN_WIDTH), D_MODEL ** -0.5),
        'b_gate': nrm(ks[4], (DEPTH, N_BRANCH, D_MODEL), 0.02),
        'q_norm_w': 1.0 + nrm(ks[5], (DEPTH, HEAD_DIM), 0.02),
        'k_norm_w': 1.0 + nrm(ks[6], (DEPTH, HEAD_DIM), 0.02),
        'w_o_attn': nrm(ks[7], (DEPTH, ATTN_OUT_WIDTH, D_MODEL), ATTN_OUT_WIDTH ** -0.5),
        'conv_w': nrm(ks[8], (DEPTH, CONV_WIDTH, CONV_CH), CONV_WIDTH ** -0.5),
        'conv_b': nrm(ks[9], (DEPTH, CONV_CH), 0.02),
        'conv_ln_w': 1.0 + nrm(ks[10], (DEPTH, CONV_CH), 0.02),
        'conv_ln_b': nrm(ks[11], (DEPTH, CONV_CH), 0.02),
        'w_pw_conv': nrm(ks[12], (DEPTH, CONV_CH, D_MODEL), CONV_CH ** -0.5),
        'w_out': nrm(ks[13], (DEPTH, D_MODEL, D_MODEL), D_MODEL ** -0.5),
        'norm2_w': 1.0 + nrm(ks[14], (DEPTH, D_MODEL), 0.02),
        'w_ffn_in': nrm(ks[15], (DEPTH, D_MODEL, 2 * D_FF), D_MODEL ** -0.5),
        'w_ffn_out': nrm(ks[16], (DEPTH, D_FF, D_MODEL), D_FF ** -0.5),
    }


def _fwd_reference(x, positions, norm1_w, w_in, b_gate, q_norm_w, k_norm_w, w_o_attn,
              conv_w, conv_b, conv_ln_w, conv_ln_b, w_pw_conv, w_out, norm2_w,
              w_ffn_in, w_ffn_out):
    B, S, _ = x.shape
    inv_freq = ROPE_THETA ** (-jnp.arange(0, ROT_DIM, 2, dtype=jnp.float32) / ROT_DIM)
    ang = positions.astype(jnp.float32)[..., None] * inv_freq
    cos = jnp.cos(ang)[:, :, None, None, :]
    sin = jnp.sin(ang)[:, :, None, None, :]
    split_at = [ATTN_QKV_WIDTH, 2 * ATTN_QKV_WIDTH, 3 * ATTN_QKV_WIDTH,
                3 * ATTN_QKV_WIDTH + 2 * CONV_CH]

    for l in range(DEPTH):
        h = rmsnorm(x, norm1_w[l])
        proj = h @ w_in[l].astype(h.dtype)
        q, k, v, conv_in, gate_logits = jnp.split(proj, split_at, axis=-1)
        hshape = (B, S, N_GROUPS, N_SLOT_HEADS, HEAD_DIM)
        q = partial_rope(rmsnorm(q.reshape(hshape), q_norm_w[l]), cos, sin)
        k = partial_rope(rmsnorm(k.reshape(hshape), k_norm_w[l]), cos, sin)
        v = v.reshape(hshape)

        outs, lses = [], []
        for g, (window, dilation) in enumerate(DILATED_PATTERNS):
            o_g, lse_g = dilated_window_attention(q[:, :, g], k[:, :, g], v[:, :, g],
                                                  dilation, window // (2 * dilation))
            outs.append(o_g)
            lses.append(lse_g)
        mix = jax.nn.softmax(jnp.stack(lses, axis=0), axis=0)
        attn = jnp.sum(mix[..., None] * jnp.stack(outs, axis=0), axis=0)
        attn = attn.reshape(B, S, ATTN_OUT_WIDTH).astype(x.dtype)
        y_a = attn @ w_o_attn[l].astype(x.dtype)

        a, b = jnp.split(conv_in, 2, axis=-1)
        u = a * jax.nn.sigmoid(b)
        u = depthwise_conv(u, conv_w[l], conv_b[l])
        u = jax.nn.silu(layernorm(u, conv_ln_w[l], conv_ln_b[l]))
        y_b = u @ w_pw_conv[l].astype(u.dtype)

        gates = jax.nn.sigmoid(gate_logits + b_gate[l].reshape(N_BRANCH * D_MODEL).astype(x.dtype))
        g_a, g_b = jnp.split(gates, 2, axis=-1)
        x = x + (g_a * y_a + g_b * y_b) @ w_out[l].astype(x.dtype)

        h2 = rmsnorm(x, norm2_w[l])
        gt, up = jnp.split(h2 @ w_ffn_in[l].astype(h2.dtype), 2, axis=-1)
        x = x + (jax.nn.silu(gt) * up) @ w_ffn_out[l].astype(x.dtype)
    return x


import jax as _jax
import jax.numpy as _jnp

TWIN_FORMAT = 'train_step'
FWD_PARAMS = ['x', 'positions', 'norm1_w', 'w_in', 'b_gate', 'q_norm_w', 'k_norm_w', 'w_o_attn', 'conv_w', 'conv_b', 'conv_ln_w', 'conv_ln_b', 'w_pw_conv', 'w_out', 'norm2_w', 'w_ffn_in', 'w_ffn_out']
TWIN_WEIGHTS = ['norm1_w', 'w_in', 'b_gate', 'q_norm_w', 'k_norm_w', 'w_o_attn', 'conv_w', 'conv_b', 'conv_ln_w', 'conv_ln_b', 'w_pw_conv', 'w_out', 'norm2_w', 'w_ffn_in', 'w_ffn_out']
TWIN_DIFF_INPUT = 'x'
TWIN_INPUTS = ['x', 'positions', 'norm1_w', 'w_in', 'b_gate', 'q_norm_w', 'k_norm_w', 'w_o_attn', 'conv_w', 'conv_b', 'conv_ln_w', 'conv_ln_b', 'w_pw_conv', 'w_out', 'norm2_w', 'w_ffn_in', 'w_ffn_out', 'loss_target', 'm_norm1_w', 'm_w_in', 'm_b_gate', 'm_q_norm_w', 'm_k_norm_w', 'm_w_o_attn', 'm_conv_w', 'm_conv_b', 'm_conv_ln_w', 'm_conv_ln_b', 'm_w_pw_conv', 'm_w_out', 'm_norm2_w', 'm_w_ffn_in', 'm_w_ffn_out', 'v_norm1_w', 'v_w_in', 'v_b_gate', 'v_q_norm_w', 'v_k_norm_w', 'v_w_o_attn', 'v_conv_w', 'v_conv_b', 'v_conv_ln_w', 'v_conv_ln_b', 'v_w_pw_conv', 'v_w_out', 'v_norm2_w', 'v_w_ffn_in', 'v_w_ffn_out']
TWIN_OUTPUTS = ['loss', 'grad_x', 'grad_norm1_w', 'grad_w_in', 'grad_b_gate', 'grad_q_norm_w', 'grad_k_norm_w', 'grad_w_o_attn', 'grad_conv_w', 'grad_conv_b', 'grad_conv_ln_w', 'grad_conv_ln_b', 'grad_w_pw_conv', 'grad_w_out', 'grad_norm2_w', 'grad_w_ffn_in', 'grad_w_ffn_out', 'delta_norm1_w', 'delta_w_in', 'delta_b_gate', 'delta_q_norm_w', 'delta_k_norm_w', 'delta_w_o_attn', 'delta_conv_w', 'delta_conv_b', 'delta_conv_ln_w', 'delta_conv_ln_b', 'delta_w_pw_conv', 'delta_w_out', 'delta_norm2_w', 'delta_w_ffn_in', 'delta_w_ffn_out', 'new_m_norm1_w', 'new_m_w_in', 'new_m_b_gate', 'new_m_q_norm_w', 'new_m_k_norm_w', 'new_m_w_o_attn', 'new_m_conv_w', 'new_m_conv_b', 'new_m_conv_ln_w', 'new_m_conv_ln_b', 'new_m_w_pw_conv', 'new_m_w_out', 'new_m_norm2_w', 'new_m_w_ffn_in', 'new_m_w_ffn_out', 'new_v_norm1_w', 'new_v_w_in', 'new_v_b_gate', 'new_v_q_norm_w', 'new_v_k_norm_w', 'new_v_w_o_attn', 'new_v_conv_w', 'new_v_conv_b', 'new_v_conv_ln_w', 'new_v_conv_ln_b', 'new_v_w_pw_conv', 'new_v_w_out', 'new_v_norm2_w', 'new_v_w_ffn_in', 'new_v_w_ffn_out']
TWIN_LEAF_KINDS = {'loss': 'loss', 'grad_x': 'grad_x', 'grad_norm1_w': 'grad_w', 'grad_w_in': 'grad_w', 'grad_b_gate': 'grad_w', 'grad_q_norm_w': 'grad_w', 'grad_k_norm_w': 'grad_w', 'grad_w_o_attn': 'grad_w', 'grad_conv_w': 'grad_w', 'grad_conv_b': 'grad_w', 'grad_conv_ln_w': 'grad_w', 'grad_conv_ln_b': 'grad_w', 'grad_w_pw_conv': 'grad_w', 'grad_w_out': 'grad_w', 'grad_norm2_w': 'grad_w', 'grad_w_ffn_in': 'grad_w', 'grad_w_ffn_out': 'grad_w', 'delta_norm1_w': 'delta_w', 'delta_w_in': 'delta_w', 'delta_b_gate': 'delta_w', 'delta_q_norm_w': 'delta_w', 'delta_k_norm_w': 'delta_w', 'delta_w_o_attn': 'delta_w', 'delta_conv_w': 'delta_w', 'delta_conv_b': 'delta_w', 'delta_conv_ln_w': 'delta_w', 'delta_conv_ln_b': 'delta_w', 'delta_w_pw_conv': 'delta_w', 'delta_w_out': 'delta_w', 'delta_norm2_w': 'delta_w', 'delta_w_ffn_in': 'delta_w', 'delta_w_ffn_out': 'delta_w', 'new_m_norm1_w': 'new_m', 'new_m_w_in': 'new_m', 'new_m_b_gate': 'new_m', 'new_m_q_norm_w': 'new_m', 'new_m_k_norm_w': 'new_m', 'new_m_w_o_attn': 'new_m', 'new_m_conv_w': 'new_m', 'new_m_conv_b': 'new_m', 'new_m_conv_ln_w': 'new_m', 'new_m_conv_ln_b': 'new_m', 'new_m_w_pw_conv': 'new_m', 'new_m_w_out': 'new_m', 'new_m_norm2_w': 'new_m', 'new_m_w_ffn_in': 'new_m', 'new_m_w_ffn_out': 'new_m', 'new_v_norm1_w': 'new_v', 'new_v_w_in': 'new_v', 'new_v_b_gate': 'new_v', 'new_v_q_norm_w': 'new_v', 'new_v_k_norm_w': 'new_v', 'new_v_w_o_attn': 'new_v', 'new_v_conv_w': 'new_v', 'new_v_conv_b': 'new_v', 'new_v_conv_ln_w': 'new_v', 'new_v_conv_ln_b': 'new_v', 'new_v_w_pw_conv': 'new_v', 'new_v_w_out': 'new_v', 'new_v_norm2_w': 'new_v', 'new_v_w_ffn_in': 'new_v', 'new_v_w_ffn_out': 'new_v'}


def _forward(args):
    return _fwd_reference(*[args[k] for k in FWD_PARAMS])


def _output_shape():
    out = _jax.eval_shape(lambda: _forward(_fwd_setup_inputs(0)))
    return out.shape, out.dtype

N_MICROBATCH = 1
ADAM_LR = 0.001
ADAM_B1 = 0.9
ADAM_B2 = 0.999
ADAM_EPS = 1e-08
ADAM_WD = 0.01
ADAM_STEP = 10
PER_EXAMPLE_BATCH_AXIS = {'x': 0, 'positions': 0, 'loss_target': 0}
SHARED_INPUTS = []
_WEIGHT_DTYPES = {'norm1_w': _jnp.float32, 'w_in': _jnp.float32, 'b_gate': _jnp.float32, 'q_norm_w': _jnp.float32, 'k_norm_w': _jnp.float32, 'w_o_attn': _jnp.float32, 'conv_w': _jnp.float32, 'conv_b': _jnp.float32, 'conv_ln_w': _jnp.float32, 'conv_ln_b': _jnp.float32, 'w_pw_conv': _jnp.float32, 'w_out': _jnp.float32, 'norm2_w': _jnp.float32, 'w_ffn_in': _jnp.float32, 'w_ffn_out': _jnp.float32}
MOMENT_SCALE = {'norm1_w': 2.093772e-01, 'w_in': 4.176218e-02, 'b_gate': 4.358522e-01, 'q_norm_w': 3.889154e-01, 'k_norm_w': 3.874671e-01, 'w_o_attn': 2.034757e-02, 'conv_w': 1.826727e-01, 'conv_b': 2.586824e+00, 'conv_ln_w': 4.245250e+00, 'conv_ln_b': 2.965591e+00, 'w_pw_conv': 3.748435e-01, 'w_out': 3.097523e-01, 'norm2_w': 1.238502e+01, 'w_ffn_in': 1.395100e-01, 'w_ffn_out': 1.675672e-01}


def _to_microbatches(a, axis):
    t = _jnp.moveaxis(a, axis, 0)
    t = t.reshape((N_MICROBATCH, t.shape[0] // N_MICROBATCH) + t.shape[1:])
    return _jnp.moveaxis(t, 1, axis + 1)


def setup_inputs(seed: int = 0) -> dict:
    inp = _fwd_setup_inputs(seed)
    key = _jax.random.fold_in(_jax.random.key(seed), 7919)
    shape, _ = _output_shape()
    out = dict(inp)
    out["loss_target"] = _jax.random.normal(_jax.random.fold_in(key, 0), shape, _jnp.float32)
    for i, name in enumerate(TWIN_WEIGHTS):
        w = inp[name].astype(_jnp.float32)
        if MOMENT_SCALE is None:
            s = _jnp.sqrt(_jnp.mean(_jnp.square(w)) + 1e-30)
        else:
            s = MOMENT_SCALE[name]
        km, kv = _jax.random.split(_jax.random.fold_in(key, i + 1))
        out[name] = w
        out["m_" + name] = s * _jax.random.normal(km, w.shape, _jnp.float32)
        out["v_" + name] = (s * s) * _jax.random.uniform(kv, w.shape, _jnp.float32, 0.5, 1.5)
    if N_MICROBATCH > 1:
        for name, axis in PER_EXAMPLE_BATCH_AXIS.items():
            out[name] = _to_microbatches(out[name], axis)
    return {'x': out['x'], 'positions': out['positions'], 'norm1_w': out['norm1_w'], 'w_in': out['w_in'], 'b_gate': out['b_gate'], 'q_norm_w': out['q_norm_w'], 'k_norm_w': out['k_norm_w'], 'w_o_attn': out['w_o_attn'], 'conv_w': out['conv_w'], 'conv_b': out['conv_b'], 'conv_ln_w': out['conv_ln_w'], 'conv_ln_b': out['conv_ln_b'], 'w_pw_conv': out['w_pw_conv'], 'w_out': out['w_out'], 'norm2_w': out['norm2_w'], 'w_ffn_in': out['w_ffn_in'], 'w_ffn_out': out['w_ffn_out'], 'loss_target': out['loss_target'], 'm_norm1_w': out['m_norm1_w'], 'm_w_in': out['m_w_in'], 'm_b_gate': out['m_b_gate'], 'm_q_norm_w': out['m_q_norm_w'], 'm_k_norm_w': out['m_k_norm_w'], 'm_w_o_attn': out['m_w_o_attn'], 'm_conv_w': out['m_conv_w'], 'm_conv_b': out['m_conv_b'], 'm_conv_ln_w': out['m_conv_ln_w'], 'm_conv_ln_b': out['m_conv_ln_b'], 'm_w_pw_conv': out['m_w_pw_conv'], 'm_w_out': out['m_w_out'], 'm_norm2_w': out['m_norm2_w'], 'm_w_ffn_in': out['m_w_ffn_in'], 'm_w_ffn_out': out['m_w_ffn_out'], 'v_norm1_w': out['v_norm1_w'], 'v_w_in': out['v_w_in'], 'v_b_gate': out['v_b_gate'], 'v_q_norm_w': out['v_q_norm_w'], 'v_k_norm_w': out['v_k_norm_w'], 'v_w_o_attn': out['v_w_o_attn'], 'v_conv_w': out['v_conv_w'], 'v_conv_b': out['v_conv_b'], 'v_conv_ln_w': out['v_conv_ln_w'], 'v_conv_ln_b': out['v_conv_ln_b'], 'v_w_pw_conv': out['v_w_pw_conv'], 'v_w_out': out['v_w_out'], 'v_norm2_w': out['v_norm2_w'], 'v_w_ffn_in': out['v_w_ffn_in'], 'v_w_ffn_out': out['v_w_ffn_out']}


def _loss(weights, diff, rest, loss_target):
    with _jax.named_scope("forward"):
        args = {**rest, TWIN_DIFF_INPUT: diff, **{k: w.astype(_WEIGHT_DTYPES[k]) for k, w in weights.items()}}
        y = _forward(args)
    with _jax.named_scope("loss_head"):
        err = _jnp.square(y.astype(_jnp.float32) - loss_target)
        return 0.5 * _jnp.sum(_jnp.mean(err, axis=-1)) if err.ndim else 0.5 * err


def _adamw(w, g, m, v):
    m = ADAM_B1 * m + (1.0 - ADAM_B1) * g
    v = ADAM_B2 * v + (1.0 - ADAM_B2) * _jnp.square(g)
    m_hat = m / (1.0 - ADAM_B1 ** ADAM_STEP)
    v_hat = v / (1.0 - ADAM_B2 ** ADAM_STEP)
    delta = -ADAM_LR * (m_hat / (_jnp.sqrt(v_hat) + ADAM_EPS) + ADAM_WD * w)
    return delta, m, v


def reference(x, positions, norm1_w, w_in, b_gate, q_norm_w, k_norm_w, w_o_attn, conv_w, conv_b, conv_ln_w, conv_ln_b, w_pw_conv, w_out, norm2_w, w_ffn_in, w_ffn_out, loss_target, m_norm1_w, m_w_in, m_b_gate, m_q_norm_w, m_k_norm_w, m_w_o_attn, m_conv_w, m_conv_b, m_conv_ln_w, m_conv_ln_b, m_w_pw_conv, m_w_out, m_norm2_w, m_w_ffn_in, m_w_ffn_out, v_norm1_w, v_w_in, v_b_gate, v_q_norm_w, v_k_norm_w, v_w_o_attn, v_conv_w, v_conv_b, v_conv_ln_w, v_conv_ln_b, v_w_pw_conv, v_w_out, v_norm2_w, v_w_ffn_in, v_w_ffn_out):
    given = dict(x=x, positions=positions, norm1_w=norm1_w, w_in=w_in, b_gate=b_gate, q_norm_w=q_norm_w, k_norm_w=k_norm_w, w_o_attn=w_o_attn, conv_w=conv_w, conv_b=conv_b, conv_ln_w=conv_ln_w, conv_ln_b=conv_ln_b, w_pw_conv=w_pw_conv, w_out=w_out, norm2_w=norm2_w, w_ffn_in=w_ffn_in, w_ffn_out=w_ffn_out, loss_target=loss_target, m_norm1_w=m_norm1_w, m_w_in=m_w_in, m_b_gate=m_b_gate, m_q_norm_w=m_q_norm_w, m_k_norm_w=m_k_norm_w, m_w_o_attn=m_w_o_attn, m_conv_w=m_conv_w, m_conv_b=m_conv_b, m_conv_ln_w=m_conv_ln_w, m_conv_ln_b=m_conv_ln_b, m_w_pw_conv=m_w_pw_conv, m_w_out=m_w_out, m_norm2_w=m_norm2_w, m_w_ffn_in=m_w_ffn_in, m_w_ffn_out=m_w_ffn_out, v_norm1_w=v_norm1_w, v_w_in=v_w_in, v_b_gate=v_b_gate, v_q_norm_w=v_q_norm_w, v_k_norm_w=v_k_norm_w, v_w_o_attn=v_w_o_attn, v_conv_w=v_conv_w, v_conv_b=v_conv_b, v_conv_ln_w=v_conv_ln_w, v_conv_ln_b=v_conv_ln_b, v_w_pw_conv=v_w_pw_conv, v_w_out=v_w_out, v_norm2_w=v_norm2_w, v_w_ffn_in=v_w_ffn_in, v_w_ffn_out=v_w_ffn_out)
    weights = {n: given[n] for n in TWIN_WEIGHTS}
    shared = {n: given[n] for n in SHARED_INPUTS}
    per_example = {n: given[n] for n in ['x', 'positions']}
    grad_fn = _jax.value_and_grad(_loss, argnums=(0, 1))

    def one_microbatch(ex, loss_target):
        ex = dict(ex)
        diff = ex.pop(TWIN_DIFF_INPUT)
        return grad_fn(weights, diff, {**shared, **ex}, loss_target)

    if N_MICROBATCH == 1:
        loss, (grad_w, grad_x) = one_microbatch(per_example, given["loss_target"])
    else:
        def body(carry, xs):
            loss_sum, grad_sum = carry
            l_k, (gw_k, gx_k) = one_microbatch(xs[0], xs[1])
            with _jax.named_scope("update"):
                return (loss_sum + l_k, _jax.tree.map(_jnp.add, grad_sum, gw_k)), gx_k

        init = (_jnp.zeros((), _jnp.float32), _jax.tree.map(_jnp.zeros_like, weights))
        (loss, grad_w), grad_x = _jax.lax.scan(body, init, (per_example, given["loss_target"]))
    with _jax.named_scope("update"):
        delta_w, new_m, new_v = {}, {}, {}
        for n in TWIN_WEIGHTS:
            delta_w[n], new_m[n], new_v[n] = _adamw(weights[n], grad_w[n], given["m_" + n], given["v_" + n])
    return (loss, grad_x, *[grad_w[n] for n in TWIN_WEIGHTS], *[delta_w[n] for n in TWIN_WEIGHTS],
            *[new_m[n] for n in TWIN_WEIGHTS], *[new_v[n] for n in TWIN_WEIGHTS])
```

```python
import functools

import numpy as np
import jax
import jax.numpy as jnp
from jax import lax
from jax.experimental import pallas as pl
from jax.experimental.pallas import tpu as pltpu

F32 = jnp.float32
BF16 = jnp.bfloat16
MESH = pl.DeviceIdType.MESH
ANY = pl.BlockSpec(memory_space=pl.ANY)

HEAD_DIM = 64
N_SLOT_HEADS = 8
DILATIONS = (1, 4, 16)
HALF_SPAN = 64
ROPE_THETA = 500000.0
ROT_DIM = 16
CONV_WIDTH = 31
EPS = 1e-6
NEG_INF = -1e30
ADAM_LR, ADAM_B1, ADAM_B2, ADAM_EPS, ADAM_WD, ADAM_STEP = 0.001, 0.9, 0.999, 1e-08, 0.01, 10

LANES = 128
QBLK = 128
KWIN = QBLK + 2 * HALF_SPAN
VMEM_LIMIT = 48 * 1024 * 1024
N_CHIPS = 4
HIGHEST = lax.Precision.HIGHEST


def _params(**kw):
    return pltpu.CompilerParams(vmem_limit_bytes=VMEM_LIMIT, **kw)


def _matmul(a, b, *, mode, tm, tn, tk, out_dtype, name, b_blocked=False,
            out_blocked=None, residual=None):
    a_shape = a.shape
    if mode == "nn":
        m_dim, k_dim = a_shape
        n_dim = b.shape[0] * b.shape[2] if b_blocked else b.shape[1]
        rows, cols, red = m_dim, n_dim, k_dim
    elif mode == "nt":
        m_dim, n_dim = a_shape
        k_dim = b.shape[1] if b_blocked else b.shape[0]
        rows, cols, red = m_dim, k_dim, n_dim
    else:
        m_dim, k_dim = a_shape
        n_dim = b.shape[1]
        rows, cols, red = k_dim, n_dim, m_dim
    assert rows % tm == 0 and cols % tn == 0 and red % tk == 0, (name, rows, cols, red)
    ni, nj, nk = rows // tm, cols // tn, red // tk

    if mode == "nn":
        a_spec = pl.BlockSpec((tm, tk), lambda i, j, k: (i, k))
        if b_blocked:
            per = b.shape[2] // tn
            b_spec = pl.BlockSpec((None, tk, tn), lambda i, j, k: (j // per, k, j % per))
        else:
            b_spec = pl.BlockSpec((tk, tn), lambda i, j, k: (k, j))
        dims = (((1,), (0,)), ((), ()))
    elif mode == "nt":
        a_spec = pl.BlockSpec((tm, tk), lambda i, j, k: (i, k))
        if b_blocked:
            per = b.shape[2] // tk
            b_spec = pl.BlockSpec((None, tn, tk), lambda i, j, k: (k // per, j, k % per))
        else:
            b_spec = pl.BlockSpec((tn, tk), lambda i, j, k: (j, k))
        dims = (((1,), (1,)), ((), ()))
    else:
        a_spec = pl.BlockSpec((tk, tm), lambda i, j, k: (k, i))
        b_spec = pl.BlockSpec((tk, tn), lambda i, j, k: (k, j))
        dims = (((0,), (0,)), ((), ()))

    if out_blocked:
        per_o = (cols // out_blocked) // tn
        out_spec = pl.BlockSpec((None, tm, tn), lambda i, j, k: (j // per_o, i, j % per_o))
        out_shape = jax.ShapeDtypeStruct((out_blocked, rows, cols // out_blocked), out_dtype)
    else:
        out_spec = pl.BlockSpec((tm, tn), lambda i, j, k: (i, j))
        out_shape = jax.ShapeDtypeStruct((rows, cols), out_dtype)

    in_specs = [a_spec, b_spec]
    operands = [a, b]
    if residual is not None:
        in_specs.append(pl.BlockSpec((tm, tn), lambda i, j, k: (i, j)))
        operands.append(residual)
    has_res = residual is not None

    def body(*refs):
        a_ref, b_ref = refs[0], refs[1]
        res_ref = refs[2] if has_res else None
        o_ref = refs[3] if has_res else refs[2]
        prod = lax.dot_general(a_ref[...], b_ref[...], dims, preferred_element_type=F32)

        def finish(val):
            if has_res:
                val = val + res_ref[...]
            o_ref[...] = val.astype(out_dtype)

        if nk == 1:
            finish(prod)
        else:
            acc_ref = refs[-1]
            k = pl.program_id(2)

            @pl.when(k == 0)
            def _():
                acc_ref[...] = prod

            @pl.when(k > 0)
            def _():
                acc_ref[...] += prod

            @pl.when(k == nk - 1)
            def _():
                finish(acc_ref[...])

    scratch = [pltpu.VMEM((tm, tn), F32)] if nk > 1 else []
    return pl.pallas_call(
        body, name=name, grid=(ni, nj, nk), in_specs=in_specs, out_specs=out_spec,
        out_shape=out_shape, scratch_shapes=scratch, compiler_params=_params(),
    )(*operands)


def _rmsnorm_fwd(x, w, name):
    s, d = x.shape
    tm = 256

    def body(x_ref, w_ref, o_ref):
        xv = x_ref[...]
        rstd = lax.rsqrt(jnp.mean(xv * xv, axis=-1, keepdims=True) + EPS)
        o_ref[...] = (xv * rstd * w_ref[...]).astype(BF16)

    return pl.pallas_call(
        body, name=name, grid=(s // tm,),
        in_specs=[pl.BlockSpec((tm, d), lambda i: (i, 0)), pl.BlockSpec((1, d), lambda i: (0, 0))],
        out_specs=pl.BlockSpec((tm, d), lambda i: (i, 0)),
        out_shape=jax.ShapeDtypeStruct((s, d), BF16), compiler_params=_params(),
    )(x, w)


def _rmsnorm_bwd(dh, x, w, dres, name):
    s, d = x.shape
    tm = 256

    def body(dh_ref, x_ref, w_ref, dres_ref, dx_ref, dxb_ref, dw_ref):
        xv = x_ref[...]
        rstd = lax.rsqrt(jnp.mean(xv * xv, axis=-1, keepdims=True) + EPS)
        xhat = xv * rstd
        dhv = dh_ref[...]
        g = dhv * w_ref[...]
        dx = rstd * (g - xhat * jnp.mean(g * xhat, axis=-1, keepdims=True)) + dres_ref[...]
        dx_ref[...] = dx
        dxb_ref[...] = dx.astype(BF16)
        part = jnp.sum(dhv * xhat, axis=0, keepdims=True)

        @pl.when(pl.program_id(0) == 0)
        def _():
            dw_ref[...] = part

        @pl.when(pl.program_id(0) > 0)
        def _():
            dw_ref[...] += part

    row = pl.BlockSpec((tm, d), lambda i: (i, 0))
    vec = pl.BlockSpec((1, d), lambda i: (0, 0))
    return pl.pallas_call(
        body, name=name, grid=(s // tm,), in_specs=[row, row, vec, row], out_specs=[row, row, vec],
        out_shape=[jax.ShapeDtypeStruct((s, d), F32), jax.ShapeDtypeStruct((s, d), BF16),
                   jax.ShapeDtypeStruct((1, d), F32)],
        compiler_params=_params(),
    )(dh, x, w, dres)


def _rope_consts():
    lane = np.arange(LANES)
    in_head = lane % HEAD_DIM
    inv_freq = ROPE_THETA ** (-jnp.arange(0, ROT_DIM, 2, dtype=F32) / ROT_DIM)
    invf = jnp.where(jnp.asarray(in_head < ROT_DIM), jnp.tile(inv_freq, LANES // (ROT_DIM // 2)), 0.0)
    m_a = np.where(in_head < ROT_DIM // 2, -1.0, 0.0).astype(np.float32)
    m_b = np.where((in_head >= ROT_DIM // 2) & (in_head < ROT_DIM), 1.0, 0.0).astype(np.float32)
    block_diag = (lane[:, None] // HEAD_DIM == lane[None, :] // HEAD_DIM).astype(np.float32)
    return (invf.reshape(1, LANES).astype(F32), jnp.asarray(m_a).reshape(1, LANES),
            jnp.asarray(m_b).reshape(1, LANES), jnp.asarray(block_diag))


def _head_sums(v, bd):
    return jnp.dot(v, bd, precision=HIGHEST, preferred_element_type=F32)


def _qk_fwd(proj, pos_col, qw2, kw2, consts, name):
    s = proj.shape[0]
    width = 3 * N_SLOT_HEADS * HEAD_DIM
    tm = 256
    invf, m_a, m_b, bd = consts
    scale = HEAD_DIM ** -0.5

    def body(q_ref, k_ref, pos_ref, qw_ref, kw_ref, invf_ref, ma_ref, mb_ref, bd_ref, qo_ref, ko_ref):
        ang = pos_ref[...].astype(F32) * invf_ref[...]
        cos = jnp.cos(ang)
        sin = jnp.sin(ang)
        s_a = sin * ma_ref[...]
        s_b = sin * mb_ref[...]
        bdv = bd_ref[...]
        for src, w_ref, dst, sc in ((q_ref, qw_ref, qo_ref, scale), (k_ref, kw_ref, ko_ref, 1.0)):
            for cb in range(width // LANES):
                cols = slice(cb * LANES, (cb + 1) * LANES)
                t = src[:, cols]
                rstd = lax.rsqrt(_head_sums(t * t, bdv) * (1.0 / HEAD_DIM) + EPS)
                y = t * rstd * w_ref[...]
                r = y * cos + pltpu.roll(y, LANES - 8, axis=1) * s_a + pltpu.roll(y, 8, axis=1) * s_b
                dst[:, cols] = r * sc if sc != 1.0 else r

    vec = pl.BlockSpec((1, LANES), lambda i: (0, 0))
    return pl.pallas_call(
        body, name=name, grid=(s // tm,),
        in_specs=[pl.BlockSpec((tm, width), lambda i: (i, 0)), pl.BlockSpec((tm, width), lambda i: (i, 1)),
                  pl.BlockSpec((tm, 1), lambda i: (i, 0)), vec, vec, vec, vec, vec,
                  pl.BlockSpec((LANES, LANES), lambda i: (0, 0))],
        out_specs=[pl.BlockSpec((tm, width), lambda i: (i, 0))] * 2,
        out_shape=[jax.ShapeDtypeStruct((s, width), F32)] * 2, compiler_params=_params(),
    )(proj, proj, pos_col, qw2, kw2, invf, m_a, m_b, bd)


def _qk_bwd(dproj, dqn, dkn, dv, da, db, proj, pos_col, qw2, kw2, consts, name):
    s = proj.shape[0]
    width = 3 * N_SLOT_HEADS * HEAD_DIM
    ch = da.shape[1]
    out_w = 3 * width + 2 * ch
    tm = 256
    invf, m_a, m_b, bd = consts
    scale = HEAD_DIM ** -0.5

    def body(dproj_in, dq_ref, dk_ref, dv_ref, da_ref, db_ref, q_ref, k_ref, pos_ref, qw_ref, kw_ref,
             invf_ref, ma_ref, mb_ref, bd_ref, out_ref, dqw_ref, dkw_ref):
        del dproj_in
        ang = pos_ref[...].astype(F32) * invf_ref[...]
        cos = jnp.cos(ang)
        sin = jnp.sin(ang)
        s_a = sin * ma_ref[...]
        s_b = sin * mb_ref[...]
        bdv = bd_ref[...]
        first = pl.program_id(0) == 0
        for src, dsrc, w_ref, col0, dw_ref, sc in ((q_ref, dq_ref, qw_ref, 0, dqw_ref, scale),
                                                   (k_ref, dk_ref, kw_ref, width, dkw_ref, 1.0)):
            dw_acc = jnp.zeros((1, LANES), F32)
            for cb in range(width // LANES):
                cols = slice(cb * LANES, (cb + 1) * LANES)
                t = src[:, cols]
                dr = dsrc[:, cols]
                if sc != 1.0:
                    dr = dr * sc
                dy = dr * cos + pltpu.roll(dr * s_a, 8, axis=1) + pltpu.roll(dr * s_b, LANES - 8, axis=1)
                rstd = lax.rsqrt(_head_sums(t * t, bdv) * (1.0 / HEAD_DIM) + EPS)
                xhat = t * rstd
                g = dy * w_ref[...]
                dt = rstd * (g - xhat * (_head_sums(g * xhat, bdv) * (1.0 / HEAD_DIM)))
                out_ref[:, col0 + cb * LANES: col0 + (cb + 1) * LANES] = dt.astype(BF16)
                dw_acc = dw_acc + jnp.sum(dy * xhat, axis=0, keepdims=True)
            dw_acc = dw_acc + pltpu.roll(dw_acc, HEAD_DIM, axis=1)

            @pl.when(first)
            def _(dw_ref=dw_ref, dw_acc=dw_acc):
                dw_ref[...] = dw_acc

            @pl.when(jnp.logical_not(first))
            def _(dw_ref=dw_ref, dw_acc=dw_acc):
                dw_ref[...] += dw_acc
        out_ref[:, 2 * width: 3 * width] = dv_ref[...].astype(BF16)
        out_ref[:, 3 * width: 3 * width + ch] = da_ref[...]
        out_ref[:, 3 * width + ch: out_w] = db_ref[...]

    vec = pl.BlockSpec((1, LANES), lambda i: (0, 0))
    blk = lambda c: pl.BlockSpec((tm, width), lambda i: (i, c))
    cblk = pl.BlockSpec((tm, ch), lambda i: (i, 0))
    return pl.pallas_call(
        body, name=name, grid=(s // tm,),
        in_specs=[ANY, blk(0), blk(0), blk(0), cblk, cblk, blk(0), blk(1),
                  pl.BlockSpec((tm, 1), lambda i: (i, 0)), vec, vec, vec, vec, vec,
                  pl.BlockSpec((LANES, LANES), lambda i: (0, 0))],
        out_specs=[pl.BlockSpec((tm, out_w), lambda i: (i, 0)), vec, vec],
        out_shape=[jax.ShapeDtypeStruct(dproj.shape, BF16)] + [jax.ShapeDtypeStruct((1, LANES), F32)] * 2,
        input_output_aliases={0: 0}, compiler_params=_params(),
    )(dproj, dqn, dkn, dv, da, db, proj, proj, pos_col, qw2, kw2, invf, m_a, m_b, bd)


def _row_chunks(n_rows, fn, chunk=256):
    def step(i, c):
        fn(pl.ds(pl.multiple_of(i * chunk, chunk), chunk))
        return c
    lax.fori_loop(0, n_rows // chunk, step, 0)


def _to_residue_major(dst, src, s, d, dst_off=0, cast=None):
    seq = s // d
    for r in range(d):
        v = src[...] if d == 1 else src[pl.ds(r, seq, stride=d), :]
        dst[dst_off + r * seq: dst_off + (r + 1) * seq, :] = v if cast is None else v.astype(cast)


def _from_residue_major(dst, src, s, d, src_off=0):
    seq = s // d
    for r in range(d):
        v = src[src_off + r * seq: src_off + (r + 1) * seq, :]
        if d == 1:
            dst[...] = v
        else:
            dst[pl.ds(r, seq, stride=d), :] = v


def _band_mask(base, seq):
    qi = lax.broadcasted_iota(jnp.int32, (QBLK, KWIN), 0)
    kj = lax.broadcasted_iota(jnp.int32, (QBLK, KWIN), 1)
    rel = kj - HALF_SPAN - qi
    s0 = jnp.bitwise_and(base, -seq)
    lo = s0 - base + HALF_SPAN
    return (jnp.abs(rel) <= HALF_SPAN) & (kj >= lo) & (kj < lo + seq)


def _attn_fwd(qn, kn, proj, name):
    s = qn.shape[0]
    n_pairs = N_SLOT_HEADS * HEAD_DIM // LANES
    v_col0 = 2 * qn.shape[1] // LANES
    nt_dims = (((1,), (1,)), ((), ()))

    def body(q_ref, k_ref, v_ref, attn_ref, lse_ref, q_rm, k_rm, v_rm, acc_rm, m_rm, l_rm,
             acc_p, m_p, l_p, m_run, l_run, acc_run):
        g = pl.program_id(1)
        zpad = jnp.zeros((HALF_SPAN, LANES), BF16)
        for buf in (k_rm, v_rm):
            buf[0:HALF_SPAN, :] = zpad
            buf[s + HALF_SPAN: s + 2 * HALF_SPAN, :] = zpad
        lane = lax.broadcasted_iota(jnp.int32, (QBLK, LANES), 1)
        low = lane < HEAD_DIM

        for gi, d in enumerate(DILATIONS):
            @pl.when(g == gi)
            def _(gi=gi, d=d):
                seq = s // d
                _to_residue_major(q_rm, q_ref, s, d, cast=BF16)
                _to_residue_major(k_rm, k_ref, s, d, dst_off=HALF_SPAN, cast=BF16)
                _to_residue_major(v_rm, v_ref, s, d, dst_off=HALF_SPAN, cast=BF16)

                def qblock(b, carry):
                    base = pl.multiple_of(b * QBLK, QBLK)
                    q = q_rm[pl.ds(base, QBLK), :]
                    kw = k_rm[pl.ds(base, KWIN), :]
                    vw = v_rm[pl.ds(base, KWIN), :]
                    mask = _band_mask(base, seq)
                    outs = []
                    for hh in range(2):
                        qm = jnp.where(low if hh == 0 else jnp.logical_not(low), q, jnp.zeros_like(q))
                        sc = lax.dot_general(qm, kw, nt_dims, preferred_element_type=F32)
                        sc = jnp.where(mask, sc, NEG_INF)
                        m = jnp.max(sc, axis=-1, keepdims=True)
                        p = jnp.exp(sc - m)
                        l = jnp.sum(p, axis=-1, keepdims=True)
                        pv = jnp.dot(p.astype(BF16), vw, preferred_element_type=F32)
                        outs.append((m, l, pv))
                    rows = pl.ds(base, QBLK)
                    acc_rm[rows, :] = jnp.where(low, outs[0][2], outs[1][2])
                    m_rm[rows, :] = jnp.where(low, outs[0][0], outs[1][0])
                    l_rm[rows, :] = jnp.where(low, outs[0][1], outs[1][1])
                    return carry

                lax.fori_loop(0, s // QBLK, qblock, 0)
                if d == 1:
                    src = (acc_rm, m_rm, l_rm)
                else:
                    for dst_, src_ in ((acc_p, acc_rm), (m_p, m_rm), (l_p, l_rm)):
                        _from_residue_major(dst_, src_, s, d)
                    src = (acc_p, m_p, l_p)

                def combine(rows):
                    a_g, m_g, l_g = src[0][rows, :], src[1][rows, :], src[2][rows, :]
                    if gi == 0:
                        m_new, l_new, a_new = m_g, l_g, a_g
                    else:
                        m_old = m_run[rows, :]
                        m_new = jnp.maximum(m_old, m_g)
                        w_old = jnp.exp(m_old - m_new)
                        w_g = jnp.exp(m_g - m_new)
                        l_new = l_run[rows, :] * w_old + l_g * w_g
                        a_new = acc_run[rows, :] * w_old + a_g * w_g
                    if gi == len(DILATIONS) - 1:
                        attn_ref[rows, :] = a_new / l_new
                        lse_ref[rows, :] = m_new + jnp.log(l_new)
                    else:
                        m_run[rows, :] = m_new
                        l_run[rows, :] = l_new
                        acc_run[rows, :] = a_new

                _row_chunks(s, combine)

    qk_spec = pl.BlockSpec((s, LANES), lambda hp, g: (0, g * n_pairs + hp))
    v_spec = pl.BlockSpec((s, LANES), lambda hp, g: (0, v_col0 + g * n_pairs + hp))
    o_spec = pl.BlockSpec((s, LANES), lambda hp, g: (0, hp))
    f32buf = pltpu.VMEM((s, LANES), F32)
    return pl.pallas_call(
        body, name=name, grid=(n_pairs, len(DILATIONS)), in_specs=[qk_spec, qk_spec, v_spec],
        out_specs=[o_spec, o_spec],
        out_shape=[jax.ShapeDtypeStruct((s, n_pairs * LANES), F32)] * 2,
        scratch_shapes=[pltpu.VMEM((s, LANES), BF16), pltpu.VMEM((s + 2 * HALF_SPAN, LANES), BF16),
                        pltpu.VMEM((s + 2 * HALF_SPAN, LANES), BF16)] + [f32buf] * 9,
        compiler_params=_params(),
    )(qn, kn, proj)


def _attn_bwd(qn, kn, proj, dattn, attn, lse, bd, name):
    s = qn.shape[0]
    n_pairs = N_SLOT_HEADS * HEAD_DIM // LANES
    v_col0 = 2 * qn.shape[1] // LANES
    nt_dims = (((1,), (1,)), ((), ()))
    tn_dims = (((0,), (0,)), ((), ()))
    spad = s + 2 * HALF_SPAN

    def body(q_ref, k_ref, v_ref, do_ref, o_ref, lse_ref, bd_ref, dq_ref, dk_ref, dv_ref,
             q_rm, k_rm, v_rm, do_rm, lse_rm, dd_rm, dq_rm, dk_rm, dv_rm, dd_p):
        g = pl.program_id(1)
        zpad = jnp.zeros((HALF_SPAN, LANES), BF16)
        for buf in (k_rm, v_rm):
            buf[0:HALF_SPAN, :] = zpad
            buf[s + HALF_SPAN: spad, :] = zpad
        zf = jnp.zeros((HALF_SPAN, LANES), F32)
        for buf in (dk_rm, dv_rm):
            buf[0:HALF_SPAN, :] = zf
            buf[s + HALF_SPAN: spad, :] = zf

        def zero_rows(rows):
            z = jnp.zeros((rows.size, LANES), F32)
            dk_rm[pl.ds(rows.start + HALF_SPAN, rows.size), :] = z
            dv_rm[pl.ds(rows.start + HALF_SPAN, rows.size), :] = z
            dd_p[rows, :] = _head_sums(do_ref[rows, :] * o_ref[rows, :], bd_ref[...])

        _row_chunks(s, zero_rows)
        lane = lax.broadcasted_iota(jnp.int32, (QBLK, LANES), 1)
        low = lane < HEAD_DIM

        for gi, d in enumerate(DILATIONS):
            @pl.when(g == gi)
            def _(d=d):
                seq = s // d
                _to_residue_major(q_rm, q_ref, s, d, cast=BF16)
                _to_residue_major(k_rm, k_ref, s, d, dst_off=HALF_SPAN, cast=BF16)
                _to_residue_major(v_rm, v_ref, s, d, dst_off=HALF_SPAN, cast=BF16)
                _to_residue_major(do_rm, do_ref, s, d, cast=BF16)
                _to_residue_major(lse_rm, lse_ref, s, d)
                _to_residue_major(dd_rm, dd_p, s, d)

                def qblock(b, carry):
                    base = pl.multiple_of(b * QBLK, QBLK)
                    rows = pl.ds(base, QBLK)
                    win = pl.ds(base, KWIN)
                    q = q_rm[rows, :]
                    do = do_rm[rows, :]
                    kw = k_rm[win, :]
                    vw = v_rm[win, :]
                    lse_b = lse_rm[rows, :]
                    dd_b = dd_rm[rows, :]
                    mask = _band_mask(base, seq)
                    dq_parts = []
                    dk_acc = jnp.zeros((KWIN, LANES), F32)
                    dv_acc = jnp.zeros((KWIN, LANES), F32)
                    for hh in range(2):
                        sel = low if hh == 0 else jnp.logical_not(low)
                        qm = jnp.where(sel, q, jnp.zeros_like(q))
                        dom = jnp.where(sel, do, jnp.zeros_like(do))
                        c0 = hh * HEAD_DIM
                        sc = lax.dot_general(qm, kw, nt_dims, preferred_element_type=F32)
                        sc = jnp.where(mask, sc, NEG_INF)
                        p = jnp.exp(sc - lse_b[:, c0:c0 + 1])
                        dp = lax.dot_general(dom, vw, nt_dims, preferred_element_type=F32)
                        ds = (p * (dp - dd_b[:, c0:c0 + 1])).astype(BF16)
                        dq_parts.append(jnp.dot(ds, kw, preferred_element_type=F32))
                        dk_acc = dk_acc + lax.dot_general(ds, qm, tn_dims, preferred_element_type=F32)
                        dv_acc = dv_acc + lax.dot_general(p.astype(BF16), dom, tn_dims,
                                                          preferred_element_type=F32)
                    dq_rm[rows, :] = jnp.where(low, dq_parts[0], dq_parts[1])
                    dk_rm[win, :] += dk_acc
                    dv_rm[win, :] += dv_acc
                    return carry

                lax.fori_loop(0, s // QBLK, qblock, 0)
                _from_residue_major(dq_ref, dq_rm, s, d)
                _from_residue_major(dk_ref, dk_rm, s, d, src_off=HALF_SPAN)
                _from_residue_major(dv_ref, dv_rm, s, d, src_off=HALF_SPAN)

    qk_spec = pl.BlockSpec((s, LANES), lambda hp, g: (0, g * n_pairs + hp))
    v_spec = pl.BlockSpec((s, LANES), lambda hp, g: (0, v_col0 + g * n_pairs + hp))
    o_spec = pl.BlockSpec((s, LANES), lambda hp, g: (0, hp))
    width = qn.shape[1]
    f32buf = pltpu.VMEM((s, LANES), F32)
    f32pad = pltpu.VMEM((spad, LANES), F32)
    return pl.pallas_call(
        body, name=name, grid=(n_pairs, len(DILATIONS)),
        in_specs=[qk_spec, qk_spec, v_spec, o_spec, o_spec, o_spec,
                  pl.BlockSpec((LANES, LANES), lambda hp, g: (0, 0))],
        out_specs=[qk_spec, qk_spec, qk_spec],
        out_shape=[jax.ShapeDtypeStruct((s, width), F32)] * 3,
        scratch_shapes=[pltpu.VMEM((s, LANES), BF16), pltpu.VMEM((spad, LANES), BF16),
                        pltpu.VMEM((spad, LANES), BF16), pltpu.VMEM((s, LANES), BF16),
                        f32buf, f32buf, f32buf, f32pad, f32pad, f32buf],
        compiler_params=_params(),
    )(qn, kn, proj, dattn, attn, lse, bd)


CONV_PAD = 16


def _conv_fwd(proj, conv_w, conv_b, col0, name):
    s = proj.shape[0]
    ch = conv_w.shape[1]
    nblk = ch // LANES
    a0 = col0 // LANES
    tr = 256
    shift = CONV_PAD - (CONV_WIDTH - 1) // 2

    def body(a_ref, b_ref, w_ref, bias_ref, u0_ref, uc_ref, pad):
        z = jnp.zeros((CONV_PAD, LANES), F32)
        pad[0:CONV_PAD, :] = z
        pad[s + CONV_PAD: s + 2 * CONV_PAD, :] = z

        def glu(rows):
            u0 = a_ref[rows, :] * jax.nn.sigmoid(b_ref[rows, :])
            u0_ref[rows, :] = u0
            pad[pl.ds(rows.start + CONV_PAD, rows.size), :] = u0

        _row_chunks(s, glu)
        for t in range(0, s, tr):
            acc = jnp.broadcast_to(bias_ref[...], (tr, LANES))
            for k in range(CONV_WIDTH):
                acc = acc + w_ref[k:k + 1, :] * pad[t + k + shift: t + k + shift + tr, :]
            uc_ref[t:t + tr, :] = acc

    return pl.pallas_call(
        body, name=name, grid=(nblk,),
        in_specs=[pl.BlockSpec((s, LANES), lambda c: (0, a0 + c)),
                  pl.BlockSpec((s, LANES), lambda c: (0, a0 + nblk + c)),
                  pl.BlockSpec((CONV_WIDTH, LANES), lambda c: (0, c)),
                  pl.BlockSpec((1, LANES), lambda c: (0, c))],
        out_specs=[pl.BlockSpec((s, LANES), lambda c: (0, c))] * 2,
        out_shape=[jax.ShapeDtypeStruct((s, ch), F32)] * 2,
        scratch_shapes=[pltpu.VMEM((s + 2 * CONV_PAD, LANES), F32)], compiler_params=_params(),
    )(proj, proj, conv_w, conv_b)


def _ln_silu_fwd(uc, ln_w, ln_b, name):
    s, ch = uc.shape
    tm = 256

    def body(u_ref, w_ref, b_ref, o_ref):
        u = u_ref[...]
        mu = jnp.mean(u, axis=-1, keepdims=True)
        xc = u - mu
        rstd = lax.rsqrt(jnp.mean(xc * xc, axis=-1, keepdims=True) + EPS)
        z = xc * rstd * w_ref[...] + b_ref[...]
        o_ref[...] = (z * jax.nn.sigmoid(z)).astype(BF16)

    row = pl.BlockSpec((tm, ch), lambda i: (i, 0))
    vec = pl.BlockSpec((1, ch), lambda i: (0, 0))
    return pl.pallas_call(
        body, name=name, grid=(s // tm,), in_specs=[row, vec, vec], out_specs=row,
        out_shape=jax.ShapeDtypeStruct((s, ch), BF16), compiler_params=_params(),
    )(uc, ln_w, ln_b)


def _ln_silu_bwd(du3, uc, ln_w, ln_b, name):
    s, ch = uc.shape
    tm = 256

    def body(d_ref, u_ref, w_ref, b_ref, du_ref, dw_ref, db_ref):
        u = u_ref[...]
        mu = jnp.mean(u, axis=-1, keepdims=True)
        xc = u - mu
        rstd = lax.rsqrt(jnp.mean(xc * xc, axis=-1, keepdims=True) + EPS)
        xhat = xc * rstd
        z = xhat * w_ref[...] + b_ref[...]
        sg = jax.nn.sigmoid(z)
        dz = d_ref[...] * (sg * (1.0 + z * (1.0 - sg)))
        dxh = dz * w_ref[...]
        du_ref[...] = rstd * (dxh - jnp.mean(dxh, axis=-1, keepdims=True)
                              - xhat * jnp.mean(dxh * xhat, axis=-1, keepdims=True))
        pw = jnp.sum(dz * xhat, axis=0, keepdims=True)
        pb = jnp.sum(dz, axis=0, keepdims=True)
        first = pl.program_id(0) == 0

        @pl.when(first)
        def _():
            dw_ref[...] = pw
            db_ref[...] = pb

        @pl.when(jnp.logical_not(first))
        def _():
            dw_ref[...] += pw
            db_ref[...] += pb

    row = pl.BlockSpec((tm, ch), lambda i: (i, 0))
    vec = pl.BlockSpec((1, ch), lambda i: (0, 0))
    return pl.pallas_call(
        body, name=name, grid=(s // tm,), in_specs=[row, row, vec, vec], out_specs=[row, vec, vec],
        out_shape=[jax.ShapeDtypeStruct((s, ch), F32), jax.ShapeDtypeStruct((1, ch), F32),
                   jax.ShapeDtypeStruct((1, ch), F32)],
        compiler_params=_params(),
    )(du3, uc, ln_w, ln_b)


def _conv_bwd(duc, u0, proj, conv_w, col0, name):
    s = proj.shape[0]
    ch = conv_w.shape[1]
    nblk = ch // LANES
    a0 = col0 // LANES
    tr = 256
    half = (CONV_WIDTH - 1) // 2
    shift = CONV_PAD - half

    def body(duc_ref, u0_ref, a_ref, b_ref, w_ref, da_ref, db_ref, dw_ref, dbias_ref, pad_d, pad_u):
        z = jnp.zeros((CONV_PAD, LANES), F32)
        for buf in (pad_d, pad_u):
            buf[0:CONV_PAD, :] = z
            buf[s + CONV_PAD: s + 2 * CONV_PAD, :] = z

        def fill(rows):
            dst = pl.ds(rows.start + CONV_PAD, rows.size)
            pad_d[dst, :] = duc_ref[rows, :]
            pad_u[dst, :] = u0_ref[rows, :]

        _row_chunks(s, fill)
        dw_acc = [jnp.zeros((8, LANES), F32) for _ in range(CONV_WIDTH)]
        dbias_acc = jnp.zeros((8, LANES), F32)
        for t in range(0, s, tr):
            d_t = duc_ref[t:t + tr, :]
            dbias_acc = dbias_acc + jnp.sum(d_t.reshape(tr // 8, 8, LANES), axis=0)
            du0 = jnp.zeros((tr, LANES), F32)
            for k in range(CONV_WIDTH):
                du0 = du0 + w_ref[k:k + 1, :] * pad_d[t - k + half + CONV_PAD: t - k + half + CONV_PAD + tr, :]
                prod = d_t * pad_u[t + k + shift: t + k + shift + tr, :]
                dw_acc[k] = dw_acc[k] + jnp.sum(prod.reshape(tr // 8, 8, LANES), axis=0)
            av = a_ref[t:t + tr, :]
            sg = jax.nn.sigmoid(b_ref[t:t + tr, :])
            da_ref[t:t + tr, :] = (du0 * sg).astype(BF16)
            db_ref[t:t + tr, :] = (du0 * av * sg * (1.0 - sg)).astype(BF16)
        for k in range(CONV_WIDTH):
            dw_ref[k:k + 1, :] = jnp.sum(dw_acc[k], axis=0, keepdims=True)
        dbias_ref[...] = jnp.sum(dbias_acc, axis=0, keepdims=True)

    col = lambda off: pl.BlockSpec((s, LANES), lambda c: (0, off + c))
    return pl.pallas_call(
        body, name=name, grid=(nblk,),
        in_specs=[col(0), col(0), col(a0), col(a0 + nblk),
                  pl.BlockSpec((CONV_WIDTH, LANES), lambda c: (0, c))],
        out_specs=[col(0), col(0), pl.BlockSpec((CONV_WIDTH, LANES), lambda c: (0, c)),
                   pl.BlockSpec((1, LANES), lambda c: (0, c))],
        out_shape=[jax.ShapeDtypeStruct((s, ch), BF16)] * 2
        + [jax.ShapeDtypeStruct((CONV_WIDTH, ch), F32), jax.ShapeDtypeStruct((1, ch), F32)],
        scratch_shapes=[pltpu.VMEM((s + 2 * CONV_PAD, LANES), F32)] * 2, compiler_params=_params(),
    )(duc, u0, proj, proj, conv_w)


GATE_BLK = 512


def _gate_fwd(proj, bg, y_a, y_b, col0, name):
    s, d = y_a.shape
    tm = 256
    g0 = col0 // GATE_BLK
    nb = d // GATE_BLK

    def body(ga_ref, gb_ref, ba_ref, bb_ref, ya_ref, yb_ref, o_ref):
        g_a = jax.nn.sigmoid(ga_ref[...] + ba_ref[...])
        g_b = jax.nn.sigmoid(gb_ref[...] + bb_ref[...])
        o_ref[...] = (g_a * ya_ref[...] + g_b * yb_ref[...]).astype(BF16)

    act = pl.BlockSpec((tm, GATE_BLK), lambda i, j: (i, j))
    return pl.pallas_call(
        body, name=name, grid=(s // tm, nb),
        in_specs=[pl.BlockSpec((tm, GATE_BLK), lambda i, j: (i, g0 + j)),
                  pl.BlockSpec((tm, GATE_BLK), lambda i, j: (i, g0 + nb + j)),
                  pl.BlockSpec((None, 1, GATE_BLK), lambda i, j: (0, 0, j)),
                  pl.BlockSpec((None, 1, GATE_BLK), lambda i, j: (1, 0, j)), act, act],
        out_specs=act, out_shape=jax.ShapeDtypeStruct((s, d), BF16), compiler_params=_params(),
    )(proj, proj, bg, bg, y_a, y_b)


def _gate_bwd(d_mixed, proj, bg, y_a, y_b, col0, name):
    s, d = y_a.shape
    tm = 256
    g0 = col0 // GATE_BLK
    nb = d // GATE_BLK

    def body(dm_ref, gl_ref, b_ref, y_ref, dgl_ref, dy_ref, db_ref):
        gate = jax.nn.sigmoid(gl_ref[...] + b_ref[...])
        dm = dm_ref[...]
        dy_ref[...] = (dm * gate).astype(BF16)
        dgl = dm * y_ref[...] * gate * (1.0 - gate)
        dgl_ref[...] = dgl.astype(BF16)
        part = jnp.sum(dgl, axis=0, keepdims=True)
        first = pl.program_id(1) == 0

        @pl.when(first)
        def _():
            db_ref[...] = part

        @pl.when(jnp.logical_not(first))
        def _():
            db_ref[...] += part

    def call(y, branch, dproj_prev):
        def wrapped(*refs):
            if dproj_prev is not None:
                refs = refs[1:]
            body(*refs)

        ins = [pl.BlockSpec((tm, GATE_BLK), lambda j, i: (i, j)),
               pl.BlockSpec((tm, GATE_BLK), lambda j, i: (i, g0 + branch * nb + j)),
               pl.BlockSpec((None, 1, GATE_BLK), lambda j, i: (branch, 0, j)),
               pl.BlockSpec((tm, GATE_BLK), lambda j, i: (i, j))]
        ops = [d_mixed, proj, bg, y]
        alias = {}
        if dproj_prev is not None:
            ins = [ANY] + ins
            ops = [dproj_prev] + ops
            alias = {0: 0}
        return pl.pallas_call(
            wrapped, name=f"{name}_{branch}", grid=(nb, s // tm), in_specs=ins,
            out_specs=[pl.BlockSpec((tm, GATE_BLK), lambda j, i: (i, g0 + branch * nb + j)),
                       pl.BlockSpec((tm, GATE_BLK), lambda j, i: (i, j)),
                       pl.BlockSpec((1, GATE_BLK), lambda j, i: (0, j))],
            out_shape=[jax.ShapeDtypeStruct((s, proj.shape[1]), BF16), jax.ShapeDtypeStruct((s, d), BF16),
                       jax.ShapeDtypeStruct((1, d), F32)],
            input_output_aliases=alias, compiler_params=_params(),
        )(*ops)

    dproj, dy_a, db_a = call(y_a, 0, None)
    dproj, dy_b, db_b = call(y_b, 1, dproj)
    return dproj, dy_a, dy_b, db_a, db_b


def _swiglu_fwd(gu, name):
    s, w2 = gu.shape
    ff = w2 // 2
    tm = 256
    tn = ff // 2

    def body(g_ref, u_ref, o_ref):
        gt = g_ref[...]
        o_ref[...] = (gt * jax.nn.sigmoid(gt) * u_ref[...]).astype(BF16)

    return pl.pallas_call(
        body, name=name, grid=(s // tm, 2),
        in_specs=[pl.BlockSpec((tm, tn), lambda i, j: (i, j)), pl.BlockSpec((tm, tn), lambda i, j: (i, 2 + j))],
        out_specs=pl.BlockSpec((tm, tn), lambda i, j: (i, j)),
        out_shape=jax.ShapeDtypeStruct((s, ff), BF16), compiler_params=_params(),
    )(gu, gu)


def _swiglu_bwd(gu, d_act, name):
    s, w2 = gu.shape
    ff = w2 // 2
    tm = 256
    tn = ff // 2

    def body(g_ref, u_ref, d_ref, o_ref):
        gt = g_ref[...]
        sg = jax.nn.sigmoid(gt)
        dv = d_ref[...]
        is_gate = pl.program_id(1) < 2

        @pl.when(is_gate)
        def _():
            o_ref[...] = (dv * u_ref[...] * (sg * (1.0 + gt * (1.0 - sg)))).astype(BF16)

        @pl.when(jnp.logical_not(is_gate))
        def _():
            o_ref[...] = (dv * gt * sg).astype(BF16)

    return pl.pallas_call(
        body, name=name, grid=(s // tm, 4),
        in_specs=[pl.BlockSpec((tm, tn), lambda i, j: (i, j % 2)),
                  pl.BlockSpec((tm, tn), lambda i, j: (i, 2 + j % 2)),
                  pl.BlockSpec((tm, tn), lambda i, j: (i, j % 2))],
        out_specs=pl.BlockSpec((tm, tn), lambda i, j: (i, j)),
        out_shape=jax.ShapeDtypeStruct((s, w2), BF16), compiler_params=_params(),
    )(gu, gu, d_act)


def _loss_fwd_bwd(y, target, name):
    s, d = y.shape
    tm = 256

    def body(y_ref, t_ref, dy_ref, dyb_ref, loss_ref, acc):
        diff = y_ref[...] - t_ref[...]
        dy = diff * (1.0 / d)
        dy_ref[...] = dy
        dyb_ref[...] = dy.astype(BF16)
        part = jnp.sum((diff * diff).reshape(tm // 8, 8, d), axis=0)
        i = pl.program_id(0)

        @pl.when(i == 0)
        def _():
            acc[...] = part

        @pl.when(i > 0)
        def _():
            acc[...] += part

        @pl.when(i == pl.num_programs(0) - 1)
        def _():
            loss_ref[...] = (0.5 / d) * jnp.sum(jnp.sum(acc[...], axis=1, keepdims=True), axis=0, keepdims=True)

    row = pl.BlockSpec((tm, d), lambda i: (i, 0))
    return pl.pallas_call(
        body, name=name, grid=(s // tm,), in_specs=[row, row],
        out_specs=[row, row, pl.BlockSpec((1, 1), lambda i: (0, 0))],
        out_shape=[jax.ShapeDtypeStruct((s, d), F32), jax.ShapeDtypeStruct((s, d), BF16),
                   jax.ShapeDtypeStruct((1, 1), F32)],
        scratch_shapes=[pltpu.VMEM((8, d), F32)], compiler_params=_params(),
    )(y, target)


def _local_step(x, pos_col, target, wts):
    consts = _rope_consts()
    bd = consts[3]
    qw2 = jnp.tile(wts["q_norm_w"], (1, LANES // HEAD_DIM))
    kw2 = jnp.tile(wts["k_norm_w"], (1, LANES // HEAD_DIM))
    qkv_w = 3 * N_SLOT_HEADS * HEAD_DIM
    conv_col0 = 3 * qkv_w
    ch = wts["conv_w"].shape[1]
    gate_col0 = conv_col0 + 2 * ch

    h = _rmsnorm_fwd(x, wts["norm1_w"], "rms1_fwd")
    proj = _matmul(h, wts["w_in"], mode="nn", tm=1024, tn=640, tk=1024, out_dtype=F32, name="mm_proj",
                   b_blocked=True)
    qn, kn = _qk_fwd(proj, pos_col, qw2, kw2, consts, "qk_fwd")
    attn, lse = _attn_fwd(qn, kn, proj, "attn_fwd")
    attn_b = attn.astype(BF16)
    y_a = _matmul(attn_b, wts["w_o_attn"], mode="nn", tm=1024, tn=256, tk=512, out_dtype=F32, name="mm_ya",
                  b_blocked=True)
    u0, uc = _conv_fwd(proj, wts["conv_w"], wts["conv_b"], conv_col0, "conv_fwd")
    u3 = _ln_silu_fwd(uc, wts["conv_ln_w"], wts["conv_ln_b"], "ln_fwd")
    y_b = _matmul(u3, wts["w_pw_conv"], mode="nn", tm=1024, tn=256, tk=512, out_dtype=F32, name="mm_yb",
                  b_blocked=True)
    mixed = _gate_fwd(proj, wts["b_gate"], y_a, y_b, gate_col0, "gate_fwd")
    x1 = _matmul(mixed, wts["w_out"], mode="nn", tm=1024, tn=512, tk=1024, out_dtype=F32, name="mm_x1",
                 residual=x)
    h2 = _rmsnorm_fwd(x1, wts["norm2_w"], "rms2_fwd")
    gu = _matmul(h2, wts["w_ffn_in"], mode="nn", tm=512, tn=1408, tk=1024, out_dtype=F32, name="mm_gu",
                 b_blocked=True)
    act = _swiglu_fwd(gu, "swiglu_fwd")
    x2 = _matmul(act, wts["w_ffn_out"], mode="nn", tm=512, tn=512, tk=2816, out_dtype=F32, name="mm_x2",
                 residual=x1)
    dy, dy_b16, loss = _loss_fwd_bwd(x2, target, "loss")

    g = {}
    d_act = _matmul(dy_b16, wts["w_ffn_out"], mode="nt", tm=512, tn=1408, tk=1024, out_dtype=F32, name="mm_dact")
    g["w_ffn_out"] = _matmul(act, dy_b16, mode="tn", tm=1408, tn=512, tk=2048, out_dtype=F32, name="mm_dwffnout")
    dgu = _swiglu_bwd(gu, d_act, "swiglu_bwd")
    dh2 = _matmul(dgu, wts["w_ffn_in"], mode="nt", tm=512, tn=1024, tk=1408, out_dtype=F32, name="mm_dh2",
                  b_blocked=True)
    g["w_ffn_in"] = _matmul(h2, dgu, mode="tn", tm=512, tn=1408, tk=2048, out_dtype=F32, name="mm_dwffnin",
                            out_blocked=N_CHIPS)
    dx1, dx1_b16, g["norm2_w"] = _rmsnorm_bwd(dh2, x1, wts["norm2_w"], dy, "rms2_bwd")
    d_mixed = _matmul(dx1_b16, wts["w_out"], mode="nt", tm=1024, tn=512, tk=1024, out_dtype=F32, name="mm_dmixed")
    g["w_out"] = _matmul(mixed, dx1_b16, mode="tn", tm=512, tn=1024, tk=2048, out_dtype=F32, name="mm_dwout")
    dproj, dy_a, dy_b, db_a, db_b = _gate_bwd(d_mixed, proj, wts["b_gate"], y_a, y_b, gate_col0, "gate_bwd")
    g["b_gate"] = jnp.concatenate([db_a, db_b], axis=0)
    dattn = _matmul(dy_a, wts["w_o_attn"], mode="nt", tm=1024, tn=512, tk=256, out_dtype=F32, name="mm_dattn",
                    b_blocked=True)
    g["w_o_attn"] = _matmul(attn_b, dy_a, mode="tn", tm=512, tn=256, tk=2048, out_dtype=F32, name="mm_dwo",
                            out_blocked=N_CHIPS)
    du3 = _matmul(dy_b, wts["w_pw_conv"], mode="nt", tm=1024, tn=512, tk=256, out_dtype=F32, name="mm_du3",
                  b_blocked=True)
    g["w_pw_conv"] = _matmul(u3, dy_b, mode="tn", tm=512, tn=256, tk=2048, out_dtype=F32, name="mm_dwpw",
                             out_blocked=N_CHIPS)
    duc, g["conv_ln_w"], g["conv_ln_b"] = _ln_silu_bwd(du3, uc, wts["conv_ln_w"], wts["conv_ln_b"], "ln_bwd")
    da, db, g["conv_w"], g["conv_b"] = _conv_bwd(duc, u0, proj, wts["conv_w"], conv_col0, "conv_bwd")
    dqn, dkn, dv = _attn_bwd(qn, kn, proj, dattn, attn, lse, bd, "attn_bwd")
    dproj, dqw, dkw = _qk_bwd(dproj, dqn, dkn, dv, da, db, proj, pos_col, qw2, kw2, consts, "qk_bwd")
    g["q_norm_w"] = dqw[:, :HEAD_DIM]
    g["k_norm_w"] = dkw[:, :HEAD_DIM]
    g["w_in"] = _matmul(h, dproj, mode="tn", tm=512, tn=640, tk=2048, out_dtype=F32, name="mm_dwin",
                        out_blocked=N_CHIPS)
    dh = _matmul(dproj, wts["w_in"], mode="nt", tm=512, tn=1024, tk=1920, out_dtype=F32, name="mm_dh",
                 b_blocked=True)
    grad_x, _, g["norm1_w"] = _rmsnorm_bwd(dh, x, wts["norm1_w"], dx1, "rms1_bwd")
    return loss, grad_x, g


def _mesh_pos():
    return lax.axis_index("x"), lax.axis_index("y"), lax.axis_index("c")


def _other_chips(x, y):
    return [(1 - x, y), (x, 1 - y), (1 - x, 1 - y)]


def _allgather_shards(shards, name):
    n = len(shards)

    def body(*refs):
        ins, outs = refs[:n], refs[n:2 * n]
        send_sems, recv_sems, local_sems = refs[2 * n:]
        x, y, c = _mesh_pos()
        me = 2 * x + y
        chips = _other_chips(x, y)
        locals_, sends = [], []
        for a in range(n):
            loc = pltpu.make_async_copy(ins[a], outs[a].at[me], local_sems.at[a])
            loc.start()
            locals_.append(loc)
            for k, (px, py) in enumerate(chips):
                cp = pltpu.make_async_remote_copy(
                    src_ref=ins[a], dst_ref=outs[a].at[me], send_sem=send_sems.at[a, k],
                    recv_sem=recv_sems.at[a, k], device_id=(px, py, c), device_id_type=MESH)
                cp.start()
                sends.append(cp)
        for a in range(n):
            for k, (px, py) in enumerate(chips):
                pltpu.make_async_remote_copy(
                    src_ref=ins[a], dst_ref=outs[a].at[2 * px + py], send_sem=send_sems.at[a, k],
                    recv_sem=recv_sems.at[a, k], device_id=(px, py, c), device_id_type=MESH).wait_recv()
        for cp in sends:
            cp.wait_send()
        for loc in locals_:
            loc.wait()

    return pl.pallas_call(
        body, name=name, in_specs=[ANY] * n, out_specs=[ANY] * n,
        out_shape=[jax.ShapeDtypeStruct((N_CHIPS,) + s.shape, s.dtype) for s in shards],
        scratch_shapes=[pltpu.SemaphoreType.DMA((n, 3)), pltpu.SemaphoreType.DMA((n, 3)),
                        pltpu.SemaphoreType.DMA((n,))],
    )(*shards)


def _pair_exchange(gs, name):
    n = len(gs)

    def body(*refs):
        ins, outs = refs[:n], refs[n:2 * n]
        send_sems, recv_sems = refs[2 * n:]
        x, y, c = _mesh_pos()
        copies = []
        for a in range(n):
            cp = pltpu.make_async_remote_copy(
                src_ref=ins[a].at[:, 1 - c], dst_ref=outs[a], send_sem=send_sems.at[a],
                recv_sem=recv_sems.at[a], device_id=(x, y, 1 - c), device_id_type=MESH)
            cp.start()
            copies.append(cp)
        for cp in copies:
            cp.wait()

    return pl.pallas_call(
        body, name=name, in_specs=[ANY] * n, out_specs=[ANY] * n,
        out_shape=[jax.ShapeDtypeStruct((g.shape[0],) + g.shape[2:], g.dtype) for g in gs],
        scratch_shapes=[pltpu.SemaphoreType.DMA((n,)), pltpu.SemaphoreType.DMA((n,))],
    )(*gs)


def _add_own_half(g, recv, c_arr, name):
    nb, _, rh, cols = g.shape

    def body(c_ref, g_ref, r_ref, o_ref):
        del c_ref
        o_ref[...] = g_ref[...] + r_ref[...]

    blk = pl.BlockSpec((None, rh, cols), lambda j, c_ref: (j, 0, 0))
    return pl.pallas_call(
        body, name=name,
        grid_spec=pltpu.PrefetchScalarGridSpec(
            num_scalar_prefetch=1, grid=(nb,),
            in_specs=[pl.BlockSpec((None, None, rh, cols), lambda j, c_ref: (j, c_ref[0], 0, 0)), blk],
            out_specs=blk),
        out_shape=jax.ShapeDtypeStruct((nb, rh, cols), g.dtype), compiler_params=_params(),
    )(c_arr, g, recv)


def _chip_exchange(ps, name):
    n = len(ps)

    def body(*refs):
        ins, outs = refs[:n], refs[n:2 * n]
        send_sems, recv_sems, local_sems = refs[2 * n:]
        x, y, c = _mesh_pos()
        me = 2 * x + y
        chips = _other_chips(x, y)
        locals_, sends = [], []
        for a in range(n):
            loc = pltpu.make_async_copy(ins[a].at[me], outs[a].at[me], local_sems.at[a])
            loc.start()
            locals_.append(loc)
            for k, (px, py) in enumerate(chips):
                cp = pltpu.make_async_remote_copy(
                    src_ref=ins[a].at[2 * px + py], dst_ref=outs[a].at[me], send_sem=send_sems.at[a, k],
                    recv_sem=recv_sems.at[a, k], device_id=(px, py, c), device_id_type=MESH)
                cp.start()
                sends.append(cp)
        for a in range(n):
            for k, (px, py) in enumerate(chips):
                pltpu.make_async_remote_copy(
                    src_ref=ins[a].at[me], dst_ref=outs[a].at[2 * px + py], send_sem=send_sems.at[a, k],
                    recv_sem=recv_sems.at[a, k], device_id=(px, py, c), device_id_type=MESH).wait_recv()
        for cp in sends:
            cp.wait_send()
        for loc in locals_:
            loc.wait()

    return pl.pallas_call(
        body, name=name, in_specs=[ANY] * n, out_specs=[ANY] * n,
        out_shape=[jax.ShapeDtypeStruct(p.shape, p.dtype) for p in ps],
        scratch_shapes=[pltpu.SemaphoreType.DMA((n, 3)), pltpu.SemaphoreType.DMA((n, 3)),
                        pltpu.SemaphoreType.DMA((n,))],
    )(*ps)


def _sum_chips(gath, name):
    nb, rh, cols = gath.shape

    def body(a_ref, b_ref, c_ref, d_ref, o_ref):
        o_ref[...] = ((a_ref[...] + b_ref[...]) + c_ref[...]) + d_ref[...]

    tr = rh // 2 if (rh // 2) % 8 == 0 else rh
    specs = [pl.BlockSpec((None, tr, cols), functools.partial(lambda i, j: (j, i, 0), j=j)) for j in range(nb)]
    return pl.pallas_call(
        body, name=name, grid=(rh // tr,), in_specs=specs,
        out_specs=pl.BlockSpec((tr, cols), lambda i: (i, 0)),
        out_shape=jax.ShapeDtypeStruct((rh, cols), gath.dtype), compiler_params=_params(),
    )(gath, gath, gath, gath)


def _pair_gather(tots, name):
    n = len(tots)

    def body(*refs):
        ins, outs = refs[:n], refs[n:2 * n]
        send_sems, recv_sems, local_sems = refs[2 * n:]
        x, y, c = _mesh_pos()
        locals_, copies = [], []
        for a in range(n):
            loc = pltpu.make_async_copy(ins[a], outs[a].at[c], local_sems.at[a])
            loc.start()
            locals_.append(loc)
            cp = pltpu.make_async_remote_copy(
                src_ref=ins[a], dst_ref=outs[a].at[c], send_sem=send_sems.at[a],
                recv_sem=recv_sems.at[a], device_id=(x, y, 1 - c), device_id_type=MESH)
            cp.start()
            copies.append(cp)
        for cp in copies:
            cp.wait()
        for loc in locals_:
            loc.wait()

    return pl.pallas_call(
        body, name=name, in_specs=[ANY] * n, out_specs=[ANY] * n,
        out_shape=[jax.ShapeDtypeStruct((2,) + t.shape, t.dtype) for t in tots],
        scratch_shapes=[pltpu.SemaphoreType.DMA((n,)), pltpu.SemaphoreType.DMA((n,)),
                        pltpu.SemaphoreType.DMA((n,))],
    )(*tots)


def _small_allreduce(v, name):
    n = v.shape[0]
    n_dev = 8

    def body(v_ref, o_ref, buf, send_sems, recv_sems):
        x, y, c = _mesh_pos()
        me = 4 * x + 2 * y + c
        buf[me] = v_ref[...]
        sends = []
        peers = []
        for r in range(1, n_dev):
            px = 1 - x if r & 4 else x
            py = 1 - y if r & 2 else y
            pc = 1 - c if r & 1 else c
            peers.append((px, py, pc))
            cp = pltpu.make_async_remote_copy(
                src_ref=v_ref, dst_ref=buf.at[me], send_sem=send_sems.at[r - 1],
                recv_sem=recv_sems.at[r - 1], device_id=(px, py, pc), device_id_type=MESH)
            cp.start()
            sends.append(cp)
        for r, (px, py, pc) in enumerate(peers):
            pltpu.make_async_remote_copy(
                src_ref=v_ref, dst_ref=buf.at[4 * px + 2 * py + pc], send_sem=send_sems.at[r],
                recv_sem=recv_sems.at[r], device_id=(px, py, pc), device_id_type=MESH).wait_recv()
        for cp in sends:
            cp.wait_send()
        acc = buf[0]
        for i in range(1, n_dev):
            acc = acc + buf[i]
        o_ref[...] = acc

    vm = pl.BlockSpec(memory_space=pltpu.VMEM)
    return pl.pallas_call(
        body, name=name, in_specs=[vm], out_specs=vm, out_shape=jax.ShapeDtypeStruct(v.shape, v.dtype),
        scratch_shapes=[pltpu.VMEM((n_dev, n, LANES), F32), pltpu.SemaphoreType.DMA((n_dev - 1,)),
                        pltpu.SemaphoreType.DMA((n_dev - 1,))],
        compiler_params=_params(),
    )(v)


def _adamw_math(w, g, m, v):
    m = ADAM_B1 * m + (1.0 - ADAM_B1) * g
    v = ADAM_B2 * v + (1.0 - ADAM_B2) * (g * g)
    m_hat = m / (1.0 - ADAM_B1 ** ADAM_STEP)
    v_hat = v / (1.0 - ADAM_B2 ** ADAM_STEP)
    delta = -ADAM_LR * (m_hat / (jnp.sqrt(v_hat) + ADAM_EPS) + ADAM_WD * w)
    return delta, m, v


def _adamw(w, g, m, v, name):
    r, c = w.shape
    tr = 128 if r % 128 == 0 else 64
    assert r % tr == 0

    def body(w_ref, g_ref, m_ref, v_ref, d_ref, mo_ref, vo_ref):
        d, mn, vn = _adamw_math(w_ref[...], g_ref[...], m_ref[...], v_ref[...])
        d_ref[...] = d
        mo_ref[...] = mn
        vo_ref[...] = vn

    blk = pl.BlockSpec((tr, c), lambda i: (i, 0))
    return pl.pallas_call(
        body, name=name, grid=(r // tr,), in_specs=[blk] * 4, out_specs=[blk] * 3,
        out_shape=[jax.ShapeDtypeStruct((r, c), F32)] * 3, compiler_params=_params(),
    )(w, g, m, v)


def _adamw_small(ws, gs, ms, vs, name):
    n = len(ws)

    def body(*refs):
        w_r, g_r, m_r, v_r = refs[:n], refs[n:2 * n], refs[2 * n:3 * n], refs[3 * n:4 * n]
        d_o, m_o, v_o = refs[4 * n:5 * n], refs[5 * n:6 * n], refs[6 * n:7 * n]
        for i in range(n):
            d, mn, vn = _adamw_math(w_r[i][...], g_r[i][...], m_r[i][...], v_r[i][...])
            d_o[i][...] = d
            m_o[i][...] = mn
            v_o[i][...] = vn

    vm = pl.BlockSpec(memory_space=pltpu.VMEM)
    shapes = [jax.ShapeDtypeStruct(w.shape, F32) for w in ws]
    outs = pl.pallas_call(
        body, name=name, in_specs=[vm] * (4 * n), out_specs=[vm] * (3 * n), out_shape=shapes * 3,
        compiler_params=_params(),
    )(*ws, *gs, *ms, *vs)
    return outs[:n], outs[n:2 * n], outs[2 * n:]


BIG = ("w_in", "w_o_attn", "w_pw_conv", "w_out", "w_ffn_in", "w_ffn_out")
ROW_SHARDED = ("w_out", "w_ffn_out")
SMALL = ("norm1_w", "b_gate", "q_norm_w", "k_norm_w", "conv_w", "conv_b", "conv_ln_w", "conv_ln_b", "norm2_w")
ORDER = ("norm1_w", "w_in", "b_gate", "q_norm_w", "k_norm_w", "w_o_attn", "conv_w", "conv_b", "conv_ln_w",
         "conv_ln_b", "w_pw_conv", "w_out", "norm2_w", "w_ffn_in", "w_ffn_out")
PACK_TILE = 8 * LANES


def _pack_small(parts):
    rows = []
    for p in parts:
        flat = p.reshape(-1)
        pad = (-flat.shape[0]) % PACK_TILE
        rows.append(jnp.pad(flat, (0, pad)).reshape(-1, LANES))
    return jnp.concatenate(rows, axis=0)


def _unpack_small(packed, shapes):
    out, row = [], 0
    for shp in shapes:
        size = int(np.prod(shp))
        nrow = -(-size // PACK_TILE) * (PACK_TILE // LANES)
        out.append(packed[row:row + nrow].reshape(-1)[:size].reshape(shp))
        row += nrow
    return out


def kernel(x, positions, norm1_w, w_in, b_gate, q_norm_w, k_norm_w, w_o_attn, conv_w, conv_b, conv_ln_w, conv_ln_b, w_pw_conv, w_out, norm2_w, w_ffn_in, w_ffn_out, loss_target, m_norm1_w, m_w_in, m_b_gate, m_q_norm_w, m_k_norm_w, m_w_o_attn, m_conv_w, m_conv_b, m_conv_ln_w, m_conv_ln_b, m_w_pw_conv, m_w_out, m_norm2_w, m_w_ffn_in, m_w_ffn_out, v_norm1_w, v_w_in, v_b_gate, v_q_norm_w, v_k_norm_w, v_w_o_attn, v_conv_w, v_conv_b, v_conv_ln_w, v_conv_ln_b, v_w_pw_conv, v_w_out, v_norm2_w, v_w_ffn_in, v_w_ffn_out):
    w = dict(norm1_w=norm1_w, w_in=w_in, b_gate=b_gate, q_norm_w=q_norm_w, k_norm_w=k_norm_w, w_o_attn=w_o_attn,
             conv_w=conv_w, conv_b=conv_b, conv_ln_w=conv_ln_w, conv_ln_b=conv_ln_b, w_pw_conv=w_pw_conv,
             w_out=w_out, norm2_w=norm2_w, w_ffn_in=w_ffn_in, w_ffn_out=w_ffn_out)
    m = dict(norm1_w=m_norm1_w, w_in=m_w_in, b_gate=m_b_gate, q_norm_w=m_q_norm_w, k_norm_w=m_k_norm_w,
             w_o_attn=m_w_o_attn, conv_w=m_conv_w, conv_b=m_conv_b, conv_ln_w=m_conv_ln_w,
             conv_ln_b=m_conv_ln_b, w_pw_conv=m_w_pw_conv, w_out=m_w_out, norm2_w=m_norm2_w,
             w_ffn_in=m_w_ffn_in, w_ffn_out=m_w_ffn_out)
    v = dict(norm1_w=v_norm1_w, w_in=v_w_in, b_gate=v_b_gate, q_norm_w=v_q_norm_w, k_norm_w=v_k_norm_w,
             w_o_attn=v_w_o_attn, conv_w=v_conv_w, conv_b=v_conv_b, conv_ln_w=v_conv_ln_w,
             conv_ln_b=v_conv_ln_b, w_pw_conv=v_w_pw_conv, w_out=v_w_out, norm2_w=v_norm2_w,
             w_ffn_in=v_w_ffn_in, w_ffn_out=v_w_ffn_out)
    cx, cy, cc = _mesh_pos()
    chip = 2 * cx + cy

    shards = [w[n][0].astype(BF16) for n in BIG] + [w["conv_w"][0], w["b_gate"][0]]
    gathered = _allgather_shards(shards, "allgather_weights")
    full = dict(zip(BIG, gathered[:len(BIG)]))
    for n in ROW_SHARDED:
        full[n] = full[n].reshape(-1, full[n].shape[2])
    conv_w_full = gathered[len(BIG)].transpose(1, 0, 2).reshape(CONV_WIDTH, -1)
    b_gate_full = gathered[len(BIG) + 1].transpose(1, 0, 2).reshape(2, 1, -1)
    wts = dict(full, conv_w=conv_w_full, b_gate=b_gate_full, norm1_w=norm1_w, q_norm_w=q_norm_w,
               k_norm_w=k_norm_w, conv_b=conv_b, conv_ln_w=conv_ln_w, conv_ln_b=conv_ln_b, norm2_w=norm2_w)

    loss, grad_x, g = _local_step(x[0], positions.reshape(-1, 1), loss_target[0], wts)

    g_big = []
    for n in BIG:
        gb = g[n] if g[n].ndim == 3 else g[n].reshape(N_CHIPS, g[n].shape[0] // N_CHIPS, g[n].shape[1])
        g_big.append(gb.reshape(N_CHIPS, 2, gb.shape[1] // 2, gb.shape[2]))
    from_sibling = _pair_exchange(g_big, "grads_pair_exchange")
    c_arr = cc.reshape(1).astype(jnp.int32)
    chip_sums = [_add_own_half(gb, rv, c_arr, f"grads_pair_add_{n}") for n, gb, rv in zip(BIG, g_big, from_sibling)]
    by_chip = _chip_exchange(chip_sums, "grads_chip_exchange")
    halves = [_sum_chips(bc, f"grads_chip_sum_{n}") for n, bc in zip(BIG, by_chip)]
    both = _pair_gather(halves, "grads_pair_gather")
    grads = {n: b.reshape(-1, b.shape[2]) for n, b in zip(BIG, both)}

    small_parts = [loss] + [g[n] for n in SMALL]
    small_shapes = [p.shape for p in small_parts]
    reduced = _unpack_small(_small_allreduce(_pack_small(small_parts), "small_allreduce"), small_shapes)
    loss_total = reduced[0].reshape(())
    for n, r in zip(SMALL, reduced[1:]):
        grads[n] = r
    ch_shard = conv_w.shape[2]
    grads["conv_w"] = lax.dynamic_slice_in_dim(grads["conv_w"], chip * ch_shard, ch_shard, axis=1)
    d_shard = b_gate.shape[2]
    grads["b_gate"] = lax.dynamic_slice_in_dim(grads["b_gate"], chip * d_shard, d_shard, axis=1)

    delta, new_m, new_v = {}, {}, {}
    for n in BIG:
        delta[n], new_m[n], new_v[n] = _adamw(w[n][0], grads[n], m[n][0], v[n][0], f"adamw_{n}")
    flat2 = lambda a: a.reshape(-1, a.shape[-1])
    d_s, m_s, v_s = _adamw_small([flat2(w[n]) for n in SMALL], [flat2(grads[n]) for n in SMALL],
                                 [flat2(m[n]) for n in SMALL], [flat2(v[n]) for n in SMALL], "adamw_small")
    for i, n in enumerate(SMALL):
        delta[n], new_m[n], new_v[n] = d_s[i], m_s[i], v_s[i]

    shaped = lambda d, n: d[n].reshape(w[n].shape)
    return (loss_total, grad_x[None], *[shaped(grads, n) for n in ORDER], *[shaped(delta, n) for n in ORDER],
            *[shaped(new_m, n) for n in ORDER], *[shaped(new_v, n) for n in ORDER])
```

```python
import functools

import numpy as np
import jax
import jax.numpy as jnp
from jax import lax
from jax.experimental import pallas as pl
from jax.experimental.pallas import tpu as pltpu

F32 = jnp.float32
BF16 = jnp.bfloat16
MESH = pl.DeviceIdType.MESH
ANY = pl.BlockSpec(memory_space=pl.ANY)

HEAD_DIM = 64
N_SLOT_HEADS = 8
DILATIONS = (1, 4, 16)
HALF_SPAN = 64
ROPE_THETA = 500000.0
ROT_DIM = 16
CONV_WIDTH = 31
EPS = 1e-6
NEG_INF = -1e30
ADAM_LR, ADAM_B1, ADAM_B2, ADAM_EPS, ADAM_WD, ADAM_STEP = 0.001, 0.9, 0.999, 1e-08, 0.01, 10

LANES = 128
QBLK = 128
KWIN = QBLK + 2 * HALF_SPAN
VMEM_LIMIT = 48 * 1024 * 1024
N_CHIPS = 4
HIGHEST = lax.Precision.HIGHEST


def _params(**kw):
    return pltpu.CompilerParams(vmem_limit_bytes=VMEM_LIMIT, **kw)


def _matmul(a, b, *, mode, tm, tn, tk, out_dtype, name, b_blocked=False,
            out_blocked=None, residual=None):
    a_shape = a.shape
    if mode == "nn":
        m_dim, k_dim = a_shape
        n_dim = b.shape[0] * b.shape[2] if b_blocked else b.shape[1]
        rows, cols, red = m_dim, n_dim, k_dim
    elif mode == "nt":
        m_dim, n_dim = a_shape
        k_dim = b.shape[1] if b_blocked else b.shape[0]
        rows, cols, red = m_dim, k_dim, n_dim
    else:
        m_dim, k_dim = a_shape
        n_dim = b.shape[1]
        rows, cols, red = k_dim, n_dim, m_dim
    assert rows % tm == 0 and cols % tn == 0 and red % tk == 0, (name, rows, cols, red)
    ni, nj, nk = rows // tm, cols // tn, red // tk

    if mode == "nn":
        a_spec = pl.BlockSpec((tm, tk), lambda i, j, k: (i, k))
        if b_blocked:
            per = b.shape[2] // tn
            b_spec = pl.BlockSpec((None, tk, tn), lambda i, j, k: (j // per, k, j % per))
        else:
            b_spec = pl.BlockSpec((tk, tn), lambda i, j, k: (k, j))
        dims = (((1,), (0,)), ((), ()))
    elif mode == "nt":
        a_spec = pl.BlockSpec((tm, tk), lambda i, j, k: (i, k))
        if b_blocked:
            per = b.shape[2] // tk
            b_spec = pl.BlockSpec((None, tn, tk), lambda i, j, k: (k // per, j, k % per))
        else:
            b_spec = pl.BlockSpec((tn, tk), lambda i, j, k: (j, k))
        dims = (((1,), (1,)), ((), ()))
    else:
        a_spec = pl.BlockSpec((tk, tm), lambda i, j, k: (k, i))
        b_spec = pl.BlockSpec((tk, tn), lambda i, j, k: (k, j))
        dims = (((0,), (0,)), ((), ()))

    if out_blocked:
        per_o = (cols // out_blocked) // tn
        out_spec = pl.BlockSpec((None, tm, tn), lambda i, j, k: (j // per_o, i, j % per_o))
        out_shape = jax.ShapeDtypeStruct((out_blocked, rows, cols // out_blocked), out_dtype)
    else:
        out_spec = pl.BlockSpec((tm, tn), lambda i, j, k: (i, j))
        out_shape = jax.ShapeDtypeStruct((rows, cols), out_dtype)

    in_specs = [a_spec, b_spec]
    operands = [a, b]
    if residual is not None:
        in_specs.append(pl.BlockSpec((tm, tn), lambda i, j, k: (i, j)))
        operands.append(residual)
    has_res = residual is not None

    def body(*refs):
        a_ref, b_ref = refs[0], refs[1]
        res_ref = refs[2] if has_res else None
        o_ref = refs[3] if has_res else refs[2]
        prod = lax.dot_general(a_ref[...], b_ref[...], dims, preferred_element_type=F32)

        def finish(val):
            if has_res:
                val = val + res_ref[...]
            o_ref[...] = val.astype(out_dtype)

        if nk == 1:
            finish(prod)
        else:
            acc_ref = refs[-1]
            k = pl.program_id(2)

            @pl.when(k == 0)
            def _():
                acc_ref[...] = prod

            @pl.when(k > 0)
            def _():
                acc_ref[...] += prod

            @pl.when(k == nk - 1)
            def _():
                finish(acc_ref[...])

    scratch = [pltpu.VMEM((tm, tn), F32)] if nk > 1 else []
    return pl.pallas_call(
        body, name=name, grid=(ni, nj, nk), in_specs=in_specs, out_specs=out_spec,
        out_shape=out_shape, scratch_shapes=scratch, compiler_params=_params(),
    )(*operands)


def _rmsnorm_fwd(x, w, name):
    s, d = x.shape
    tm = 256

    def body(x_ref, w_ref, o_ref):
        xv = x_ref[...]
        rstd = lax.rsqrt(jnp.mean(xv * xv, axis=-1, keepdims=True) + EPS)
        o_ref[...] = (xv * rstd * w_ref[...]).astype(BF16)

    return pl.pallas_call(
        body, name=name, grid=(s // tm,),
        in_specs=[pl.BlockSpec((tm, d), lambda i: (i, 0)), pl.BlockSpec((1, d), lambda i: (0, 0))],
        out_specs=pl.BlockSpec((tm, d), lambda i: (i, 0)),
        out_shape=jax.ShapeDtypeStruct((s, d), BF16), compiler_params=_params(),
    )(x, w)


def _rmsnorm_bwd(dh, x, w, dres, name):
    s, d = x.shape
    tm = 256

    def body(dh_ref, x_ref, w_ref, dres_ref, dx_ref, dxb_ref, dw_ref):
        xv = x_ref[...]
        rstd = lax.rsqrt(jnp.mean(xv * xv, axis=-1, keepdims=True) + EPS)
        xhat = xv * rstd
        dhv = dh_ref[...]
        g = dhv * w_ref[...]
        dx = rstd * (g - xhat * jnp.mean(g * xhat, axis=-1, keepdims=True)) + dres_ref[...]
        dx_ref[...] = dx
        dxb_ref[...] = dx.astype(BF16)
        part = jnp.sum(dhv * xhat, axis=0, keepdims=True)

        @pl.when(pl.program_id(0) == 0)
        def _():
            dw_ref[...] = part

        @pl.when(pl.program_id(0) > 0)
        def _():
            dw_ref[...] += part

    row = pl.BlockSpec((tm, d), lambda i: (i, 0))
    vec = pl.BlockSpec((1, d), lambda i: (0, 0))
    return pl.pallas_call(
        body, name=name, grid=(s // tm,), in_specs=[row, row, vec, row], out_specs=[row, row, vec],
        out_shape=[jax.ShapeDtypeStruct((s, d), F32), jax.ShapeDtypeStruct((s, d), BF16),
                   jax.ShapeDtypeStruct((1, d), F32)],
        compiler_params=_params(),
    )(dh, x, w, dres)


def _rope_consts():
    lane = np.arange(LANES)
    in_head = lane % HEAD_DIM
    inv_freq = ROPE_THETA ** (-jnp.arange(0, ROT_DIM, 2, dtype=F32) / ROT_DIM)
    invf = jnp.where(jnp.asarray(in_head < ROT_DIM), jnp.tile(inv_freq, LANES // (ROT_DIM // 2)), 0.0)
    m_a = np.where(in_head < ROT_DIM // 2, -1.0, 0.0).astype(np.float32)
    m_b = np.where((in_head >= ROT_DIM // 2) & (in_head < ROT_DIM), 1.0, 0.0).astype(np.float32)
    block_diag = (lane[:, None] // HEAD_DIM == lane[None, :] // HEAD_DIM).astype(np.float32)
    return (invf.reshape(1, LANES).astype(F32), jnp.asarray(m_a).reshape(1, LANES),
            jnp.asarray(m_b).reshape(1, LANES), jnp.asarray(block_diag))


def _head_sums(v, bd):
    return jnp.dot(v, bd, precision=HIGHEST, preferred_element_type=F32)


def _qk_fwd(proj, pos_col, qw2, kw2, consts, name):
    s = proj.shape[0]
    width = 3 * N_SLOT_HEADS * HEAD_DIM
    tm = 256
    invf, m_a, m_b, bd = consts
    scale = HEAD_DIM ** -0.5

    def body(q_ref, k_ref, pos_ref, qw_ref, kw_ref, invf_ref, ma_ref, mb_ref, bd_ref, qo_ref, ko_ref):
        ang = pos_ref[...].astype(F32) * invf_ref[...]
        cos = jnp.cos(ang)
        sin = jnp.sin(ang)
        s_a = sin * ma_ref[...]
        s_b = sin * mb_ref[...]
        bdv = bd_ref[...]
        for src, w_ref, dst, sc in ((q_ref, qw_ref, qo_ref, scale), (k_ref, kw_ref, ko_ref, 1.0)):
            for cb in range(width // LANES):
                cols = slice(cb * LANES, (cb + 1) * LANES)
                t = src[:, cols]
                rstd = lax.rsqrt(_head_sums(t * t, bdv) * (1.0 / HEAD_DIM) + EPS)
                y = t * rstd * w_ref[...]
                r = y * cos + pltpu.roll(y, LANES - 8, axis=1) * s_a + pltpu.roll(y, 8, axis=1) * s_b
                dst[:, cols] = r * sc if sc != 1.0 else r

    vec = pl.BlockSpec((1, LANES), lambda i: (0, 0))
    return pl.pallas_call(
        body, name=name, grid=(s // tm,),
        in_specs=[pl.BlockSpec((tm, width), lambda i: (i, 0)), pl.BlockSpec((tm, width), lambda i: (i, 1)),
                  pl.BlockSpec((tm, 1), lambda i: (i, 0)), vec, vec, vec, vec, vec,
                  pl.BlockSpec((LANES, LANES), lambda i: (0, 0))],
        out_specs=[pl.BlockSpec((tm, width), lambda i: (i, 0))] * 2,
        out_shape=[jax.ShapeDtypeStruct((s, width), F32)] * 2, compiler_params=_params(),
    )(proj, proj, pos_col, qw2, kw2, invf, m_a, m_b, bd)


def _qk_bwd(dproj, dqn, dkn, dv, da, db, proj, pos_col, qw2, kw2, consts, name):
    s = proj.shape[0]
    width = 3 * N_SLOT_HEADS * HEAD_DIM
    ch = da.shape[1]
    out_w = 3 * width + 2 * ch
    tm = 256
    invf, m_a, m_b, bd = consts
    scale = HEAD_DIM ** -0.5

    def body(dproj_in, dq_ref, dk_ref, dv_ref, da_ref, db_ref, q_ref, k_ref, pos_ref, qw_ref, kw_ref,
             invf_ref, ma_ref, mb_ref, bd_ref, out_ref, dqw_ref, dkw_ref):
        del dproj_in
        ang = pos_ref[...].astype(F32) * invf_ref[...]
        cos = jnp.cos(ang)
        sin = jnp.sin(ang)
        s_a = sin * ma_ref[...]
        s_b = sin * mb_ref[...]
        bdv = bd_ref[...]
        first = pl.program_id(0) == 0
        for src, dsrc, w_ref, col0, dw_ref, sc in ((q_ref, dq_ref, qw_ref, 0, dqw_ref, scale),
                                                   (k_ref, dk_ref, kw_ref, width, dkw_ref, 1.0)):
            dw_acc = jnp.zeros((1, LANES), F32)
            for cb in range(width // LANES):
                cols = slice(cb * LANES, (cb + 1) * LANES)
                t = src[:, cols]
                dr = dsrc[:, cols]
                if sc != 1.0:
                    dr = dr * sc
                dy = dr * cos + pltpu.roll(dr * s_a, 8, axis=1) + pltpu.roll(dr * s_b, LANES - 8, axis=1)
                rstd = lax.rsqrt(_head_sums(t * t, bdv) * (1.0 / HEAD_DIM) + EPS)
                xhat = t * rstd
                g = dy * w_ref[...]
                dt = rstd * (g - xhat * (_head_sums(g * xhat, bdv) * (1.0 / HEAD_DIM)))
                out_ref[:, col0 + cb * LANES: col0 + (cb + 1) * LANES] = dt.astype(BF16)
                dw_acc = dw_acc + jnp.sum(dy * xhat, axis=0, keepdims=True)
            dw_acc = dw_acc + pltpu.roll(dw_acc, HEAD_DIM, axis=1)

            @pl.when(first)
            def _(dw_ref=dw_ref, dw_acc=dw_acc):
                dw_ref[...] = dw_acc

            @pl.when(jnp.logical_not(first))
            def _(dw_ref=dw_ref, dw_acc=dw_acc):
                dw_ref[...] += dw_acc
        out_ref[:, 2 * width: 3 * width] = dv_ref[...].astype(BF16)
        out_ref[:, 3 * width: 3 * width + ch] = da_ref[...]
        out_ref[:, 3 * width + ch: out_w] = db_ref[...]

    vec = pl.BlockSpec((1, LANES), lambda i: (0, 0))
    blk = lambda c: pl.BlockSpec((tm, width), lambda i: (i, c))
    cblk = pl.BlockSpec((tm, ch), lambda i: (i, 0))
    return pl.pallas_call(
        body, name=name, grid=(s // tm,),
        in_specs=[ANY, blk(0), blk(0), blk(0), cblk, cblk, blk(0), blk(1),
                  pl.BlockSpec((tm, 1), lambda i: (i, 0)), vec, vec, vec, vec, vec,
                  pl.BlockSpec((LANES, LANES), lambda i: (0, 0))],
        out_specs=[pl.BlockSpec((tm, out_w), lambda i: (i, 0)), vec, vec],
        out_shape=[jax.ShapeDtypeStruct(dproj.shape, BF16)] + [jax.ShapeDtypeStruct((1, LANES), F32)] * 2,
        input_output_aliases={0: 0}, compiler_params=_params(),
    )(dproj, dqn, dkn, dv, da, db, proj, proj, pos_col, qw2, kw2, invf, m_a, m_b, bd)


def _row_chunks(n_rows, fn, chunk=256):
    def step(i, c):
        fn(pl.ds(pl.multiple_of(i * chunk, chunk), chunk))
        return c
    lax.fori_loop(0, n_rows // chunk, step, 0)


def _to_residue_major(dst, src, s, d, dst_off=0, cast=None):
    seq = s // d
    for r in range(d):
        v = src[...] if d == 1 else src[pl.ds(r, seq, stride=d), :]
        dst[dst_off + r * seq: dst_off + (r + 1) * seq, :] = v if cast is None else v.astype(cast)


def _from_residue_major(dst, src, s, d, src_off=0):
    seq = s // d
    for r in range(d):
        v = src[src_off + r * seq: src_off + (r + 1) * seq, :]
        if d == 1:
            dst[...] = v
        else:
            dst[pl.ds(r, seq, stride=d), :] = v


def _band_mask(base, seq):
    qi = lax.broadcasted_iota(jnp.int32, (QBLK, KWIN), 0)
    kj = lax.broadcasted_iota(jnp.int32, (QBLK, KWIN), 1)
    rel = kj - HALF_SPAN - qi
    s0 = jnp.bitwise_and(base, -seq)
    lo = s0 - base + HALF_SPAN
    return (jnp.abs(rel) <= HALF_SPAN) & (kj >= lo) & (kj < lo + seq)


def _attn_fwd(qn, kn, proj, name):
    s = qn.shape[0]
    n_pairs = N_SLOT_HEADS * HEAD_DIM // LANES
    v_col0 = 2 * qn.shape[1] // LANES
    nt_dims = (((1,), (1,)), ((), ()))

    def body(q_ref, k_ref, v_ref, attn_ref, lse_ref, q_rm, k_rm, v_rm, acc_rm, m_rm, l_rm,
             acc_p, m_p, l_p, m_run, l_run, acc_run):
        g = pl.program_id(1)
        zpad = jnp.zeros((HALF_SPAN, LANES), BF16)
        for buf in (k_rm, v_rm):
            buf[0:HALF_SPAN, :] = zpad
            buf[s + HALF_SPAN: s + 2 * HALF_SPAN, :] = zpad
        lane = lax.broadcasted_iota(jnp.int32, (QBLK, LANES), 1)
        low = lane < HEAD_DIM

        for gi, d in enumerate(DILATIONS):
            @pl.when(g == gi)
            def _(gi=gi, d=d):
                seq = s // d
                _to_residue_major(q_rm, q_ref, s, d, cast=BF16)
                _to_residue_major(k_rm, k_ref, s, d, dst_off=HALF_SPAN, cast=BF16)
                _to_residue_major(v_rm, v_ref, s, d, dst_off=HALF_SPAN, cast=BF16)

                def qblock(b, carry):
                    base = pl.multiple_of(b * QBLK, QBLK)
                    q = q_rm[pl.ds(base, QBLK), :]
                    kw = k_rm[pl.ds(base, KWIN), :]
                    vw = v_rm[pl.ds(base, KWIN), :]
                    mask = _band_mask(base, seq)
                    outs = []
                    for hh in range(2):
                        qm = jnp.where(low if hh == 0 else jnp.logical_not(low), q, jnp.zeros_like(q))
                        sc = lax.dot_general(qm, kw, nt_dims, preferred_element_type=F32)
                        sc = jnp.where(mask, sc, NEG_INF)
                        m = jnp.max(sc, axis=-1, keepdims=True)
                        p = jnp.exp(sc - m)
                        l = jnp.sum(p, axis=-1, keepdims=True)
                        pv = jnp.dot(p.astype(BF16), vw, preferred_element_type=F32)
                        outs.append((m, l, pv))
                    rows = pl.ds(base, QBLK)
                    acc_rm[rows, :] = jnp.where(low, outs[0][2], outs[1][2])
                    m_rm[rows, :] = jnp.where(low, outs[0][0], outs[1][0])
                    l_rm[rows, :] = jnp.where(low, outs[0][1], outs[1][1])
                    return carry

                lax.fori_loop(0, s // QBLK, qblock, 0)
                if d == 1:
                    src = (acc_rm, m_rm, l_rm)
                else:
                    for dst_, src_ in ((acc_p, acc_rm), (m_p, m_rm), (l_p, l_rm)):
                        _from_residue_major(dst_, src_, s, d)
                    src = (acc_p, m_p, l_p)

                def combine(rows):
                    a_g, m_g, l_g = src[0][rows, :], src[1][rows, :], src[2][rows, :]
                    if gi == 0:
                        m_new, l_new, a_new = m_g, l_g, a_g
                    else:
                        m_old = m_run[rows, :]
                        m_new = jnp.maximum(m_old, m_g)
                        w_old = jnp.exp(m_old - m_new)
                        w_g = jnp.exp(m_g - m_new)
                        l_new = l_run[rows, :] * w_old + l_g * w_g
                        a_new = acc_run[rows, :] * w_old + a_g * w_g
                    if gi == len(DILATIONS) - 1:
                        attn_ref[rows, :] = a_new / l_new
                        lse_ref[rows, :] = m_new + jnp.log(l_new)
                    else:
                        m_run[rows, :] = m_new
                        l_run[rows, :] = l_new
                        acc_run[rows, :] = a_new

                _row_chunks(s, combine)

    qk_spec = pl.BlockSpec((s, LANES), lambda hp, g: (0, g * n_pairs + hp))
    v_spec = pl.BlockSpec((s, LANES), lambda hp, g: (0, v_col0 + g * n_pairs + hp))
    o_spec = pl.BlockSpec((s, LANES), lambda hp, g: (0, hp))
    f32buf = pltpu.VMEM((s, LANES), F32)
    return pl.pallas_call(
        body, name=name, grid=(n_pairs, len(DILATIONS)), in_specs=[qk_spec, qk_spec, v_spec],
        out_specs=[o_spec, o_spec],
        out_shape=[jax.ShapeDtypeStruct((s, n_pairs * LANES), F32)] * 2,
        scratch_shapes=[pltpu.VMEM((s, LANES), BF16), pltpu.VMEM((s + 2 * HALF_SPAN, LANES), BF16),
                        pltpu.VMEM((s + 2 * HALF_SPAN, LANES), BF16)] + [f32buf] * 9,
        compiler_params=_params(),
    )(qn, kn, proj)


def _attn_bwd(qn, kn, proj, dattn, attn, lse, bd, name):
    s = qn.shape[0]
    n_pairs = N_SLOT_HEADS * HEAD_DIM // LANES
    v_col0 = 2 * qn.shape[1] // LANES
    nt_dims = (((1,), (1,)), ((), ()))
    tn_dims = (((0,), (0,)), ((), ()))
    spad = s + 2 * HALF_SPAN

    def body(q_ref, k_ref, v_ref, do_ref, o_ref, lse_ref, bd_ref, dq_ref, dk_ref, dv_ref,
             q_rm, k_rm, v_rm, do_rm, lse_rm, dd_rm, dq_rm, dk_rm, dv_rm, dd_p):
        g = pl.program_id(1)
        zpad = jnp.zeros((HALF_SPAN, LANES), BF16)
        for buf in (k_rm, v_rm):
            buf[0:HALF_SPAN, :] = zpad
            buf[s + HALF_SPAN: spad, :] = zpad
        zf = jnp.zeros((HALF_SPAN, LANES), F32)
        for buf in (dk_rm, dv_rm):
            buf[0:HALF_SPAN, :] = zf
            buf[s + HALF_SPAN: spad, :] = zf

        def zero_rows(rows):
            z = jnp.zeros((rows.size, LANES), F32)
            dk_rm[pl.ds(rows.start + HALF_SPAN, rows.size), :] = z
            dv_rm[pl.ds(rows.start + HALF_SPAN, rows.size), :] = z
            dd_p[rows, :] = _head_sums(do_ref[rows, :] * o_ref[rows, :], bd_ref[...])

        _row_chunks(s, zero_rows)
        lane = lax.broadcasted_iota(jnp.int32, (QBLK, LANES), 1)
        low = lane < HEAD_DIM

        for gi, d in enumerate(DILATIONS):
            @pl.when(g == gi)
            def _(d=d):
                seq = s // d
                _to_residue_major(q_rm, q_ref, s, d, cast=BF16)
                _to_residue_major(k_rm, k_ref, s, d, dst_off=HALF_SPAN, cast=BF16)
                _to_residue_major(v_rm, v_ref, s, d, dst_off=HALF_SPAN, cast=BF16)
                _to_residue_major(do_rm, do_ref, s, d, cast=BF16)
                _to_residue_major(lse_rm, lse_ref, s, d)
                _to_residue_major(dd_rm, dd_p, s, d)

                def qblock(b, carry):
                    base = pl.multiple_of(b * QBLK, QBLK)
                    rows = pl.ds(base, QBLK)
                    win = pl.ds(base, KWIN)
                    q = q_rm[rows, :]
                    do = do_rm[rows, :]
                    kw = k_rm[win, :]
                    vw = v_rm[win, :]
                    lse_b = lse_rm[rows, :]
                    dd_b = dd_rm[rows, :]
                    mask = _band_mask(base, seq)
                    dq_parts = []
                    dk_acc = jnp.zeros((KWIN, LANES), F32)
                    dv_acc = jnp.zeros((KWIN, LANES), F32)
                    for hh in range(2):
                        sel = low if hh == 0 else jnp.logical_not(low)
                        qm = jnp.where(sel, q, jnp.zeros_like(q))
                        dom = jnp.where(sel, do, jnp.zeros_like(do))
                        c0 = hh * HEAD_DIM
                        sc = lax.dot_general(qm, kw, nt_dims, preferred_element_type=F32)
                        sc = jnp.where(mask, sc, NEG_INF)
                        p = jnp.exp(sc - lse_b[:, c0:c0 + 1])
                        dp = lax.dot_general(dom, vw, nt_dims, preferred_element_type=F32)
                        ds = (p * (dp - dd_b[:, c0:c0 + 1])).astype(BF16)
                        dq_parts.append(jnp.dot(ds, kw, preferred_element_type=F32))
                        dk_acc = dk_acc + lax.dot_general(ds, qm, tn_dims, preferred_element_type=F32)
                        dv_acc = dv_acc + lax.dot_general(p.astype(BF16), dom, tn_dims,
                                                          preferred_element_type=F32)
                    dq_rm[rows, :] = jnp.where(low, dq_parts[0], dq_parts[1])
                    dk_rm[win, :] += dk_acc
                    dv_rm[win, :] += dv_acc
                    return carry

                lax.fori_loop(0, s // QBLK, qblock, 0)
                _from_residue_major(dq_ref, dq_rm, s, d)
                _from_residue_major(dk_ref, dk_rm, s, d, src_off=HALF_SPAN)
                _from_residue_major(dv_ref, dv_rm, s, d, src_off=HALF_SPAN)

    qk_spec = pl.BlockSpec((s, LANES), lambda hp, g: (0, g * n_pairs + hp))
    v_spec = pl.BlockSpec((s, LANES), lambda hp, g: (0, v_col0 + g * n_pairs + hp))
    o_spec = pl.BlockSpec((s, LANES), lambda hp, g: (0, hp))
    width = qn.shape[1]
    f32buf = pltpu.VMEM((s, LANES), F32)
    f32pad = pltpu.VMEM((spad, LANES), F32)
    return pl.pallas_call(
        body, name=name, grid=(n_pairs, len(DILATIONS)),
        in_specs=[qk_spec, qk_spec, v_spec, o_spec, o_spec, o_spec,
                  pl.BlockSpec((LANES, LANES), lambda hp, g: (0, 0))],
        out_specs=[qk_spec, qk_spec, qk_spec],
        out_shape=[jax.ShapeDtypeStruct((s, width), F32)] * 3,
        scratch_shapes=[pltpu.VMEM((s, LANES), BF16), pltpu.VMEM((spad, LANES), BF16),
                        pltpu.VMEM((spad, LANES), BF16), pltpu.VMEM((s, LANES), BF16),
                        f32buf, f32buf, f32buf, f32pad, f32pad, f32buf],
        compiler_params=_params(),
    )(qn, kn, proj, dattn, attn, lse, bd)


CONV_PAD = 16


def _conv_fwd(proj, conv_w, conv_b, col0, name):
    s = proj.shape[0]
    ch = conv_w.shape[1]
    nblk = ch // LANES
    a0 = col0 // LANES
    tr = 256
    shift = CONV_PAD - (CONV_WIDTH - 1) // 2

    def body(a_ref, b_ref, w_ref, bias_ref, u0_ref, uc_ref, pad):
        z = jnp.zeros((CONV_PAD, LANES), F32)
        pad[0:CONV_PAD, :] = z
        pad[s + CONV_PAD: s + 2 * CONV_PAD, :] = z

        def glu(rows):
            u0 = a_ref[rows, :] * jax.nn.sigmoid(b_ref[rows, :])
            u0_ref[rows, :] = u0
            pad[pl.ds(rows.start + CONV_PAD, rows.size), :] = u0

        _row_chunks(s, glu)
        for t in range(0, s, tr):
            acc = jnp.broadcast_to(bias_ref[...], (tr, LANES))
            for k in range(CONV_WIDTH):
                acc = acc + w_ref[k:k + 1, :] * pad[t + k + shift: t + k + shift + tr, :]
            uc_ref[t:t + tr, :] = acc

    return pl.pallas_call(
        body, name=name, grid=(nblk,),
        in_specs=[pl.BlockSpec((s, LANES), lambda c: (0, a0 + c)),
                  pl.BlockSpec((s, LANES), lambda c: (0, a0 + nblk + c)),
                  pl.BlockSpec((CONV_WIDTH, LANES), lambda c: (0, c)),
                  pl.BlockSpec((1, LANES), lambda c: (0, c))],
        out_specs=[pl.BlockSpec((s, LANES), lambda c: (0, c))] * 2,
        out_shape=[jax.ShapeDtypeStruct((s, ch), F32)] * 2,
        scratch_shapes=[pltpu.VMEM((s + 2 * CONV_PAD, LANES), F32)], compiler_params=_params(),
    )(proj, proj, conv_w, conv_b)


def _ln_silu_fwd(uc, ln_w, ln_b, name):
    s, ch = uc.shape
    tm = 256

    def body(u_ref, w_ref, b_ref, o_ref):
        u = u_ref[...]
        mu = jnp.mean(u, axis=-1, keepdims=True)
        xc = u - mu
        rstd = lax.rsqrt(jnp.mean(xc * xc, axis=-1, keepdims=True) + EPS)
        z = xc * rstd * w_ref[...] + b_ref[...]
        o_ref[...] = (z * jax.nn.sigmoid(z)).astype(BF16)

    row = pl.BlockSpec((tm, ch), lambda i: (i, 0))
    vec = pl.BlockSpec((1, ch), lambda i: (0, 0))
    return pl.pallas_call(
        body, name=name, grid=(s // tm,), in_specs=[row, vec, vec], out_specs=row,
        out_shape=jax.ShapeDtypeStruct((s, ch), BF16), compiler_params=_params(),
    )(uc, ln_w, ln_b)


def _ln_silu_bwd(du3, uc, ln_w, ln_b, name):
    s, ch = uc.shape
    tm = 256

    def body(d_ref, u_ref, w_ref, b_ref, du_ref, dw_ref, db_ref):
        u = u_ref[...]
        mu = jnp.mean(u, axis=-1, keepdims=True)
        xc = u - mu
        rstd = lax.rsqrt(jnp.mean(xc * xc, axis=-1, keepdims=True) + EPS)
        xhat = xc * rstd
        z = xhat * w_ref[...] + b_ref[...]
        sg = jax.nn.sigmoid(z)
        dz = d_ref[...] * (sg * (1.0 + z * (1.0 - sg)))
        dxh = dz * w_ref[...]
        du_ref[...] = rstd * (dxh - jnp.mean(dxh, axis=-1, keepdims=True)
                              - xhat * jnp.mean(dxh * xhat, axis=-1, keepdims=True))
        pw = jnp.sum(dz * xhat, axis=0, keepdims=True)
        pb = jnp.sum(dz, axis=0, keepdims=True)
        first = pl.program_id(0) == 0

        @pl.when(first)
        def _():
            dw_ref[...] = pw
            db_ref[...] = pb

        @pl.when(jnp.logical_not(first))
        def _():
            dw_ref[...] += pw
            db_ref[...] += pb

    row = pl.BlockSpec((tm, ch), lambda i: (i, 0))
    vec = pl.BlockSpec((1, ch), lambda i: (0, 0))
    return pl.pallas_call(
        body, name=name, grid=(s // tm,), in_specs=[row, row, vec, vec], out_specs=[row, vec, vec],
        out_shape=[jax.ShapeDtypeStruct((s, ch), F32), jax.ShapeDtypeStruct((1, ch), F32),
                   jax.ShapeDtypeStruct((1, ch), F32)],
        compiler_params=_params(),
    )(du3, uc, ln_w, ln_b)


def _conv_bwd(duc, u0, proj, conv_w, col0, name):
    s = proj.shape[0]
    ch = conv_w.shape[1]
    nblk = ch // LANES
    a0 = col0 // LANES
    tr = 256
    half = (CONV_WIDTH - 1) // 2
    shift = CONV_PAD - half

    def body(duc_ref, u0_ref, a_ref, b_ref, w_ref, da_ref, db_ref, dw_ref, dbias_ref, pad_d, pad_u):
        z = jnp.zeros((CONV_PAD, LANES), F32)
        for buf in (pad_d, pad_u):
            buf[0:CONV_PAD, :] = z
            buf[s + CONV_PAD: s + 2 * CONV_PAD, :] = z

        def fill(rows):
            dst = pl.ds(rows.start + CONV_PAD, rows.size)
            pad_d[dst, :] = duc_ref[rows, :]
            pad_u[dst, :] = u0_ref[rows, :]

        _row_chunks(s, fill)
        dw_acc = [jnp.zeros((8, LANES), F32) for _ in range(CONV_WIDTH)]
        dbias_acc = jnp.zeros((8, LANES), F32)
        for t in range(0, s, tr):
            d_t = duc_ref[t:t + tr, :]
            dbias_acc = dbias_acc + jnp.sum(d_t.reshape(tr // 8, 8, LANES), axis=0)
            du0 = jnp.zeros((tr, LANES), F32)
            for k in range(CONV_WIDTH):
                du0 = du0 + w_ref[k:k + 1, :] * pad_d[t - k + half + CONV_PAD: t - k + half + CONV_PAD + tr, :]
                prod = d_t * pad_u[t + k + shift: t + k + shift + tr, :]
                dw_acc[k] = dw_acc[k] + jnp.sum(prod.reshape(tr // 8, 8, LANES), axis=0)
            av = a_ref[t:t + tr, :]
            sg = jax.nn.sigmoid(b_ref[t:t + tr, :])
            da_ref[t:t + tr, :] = (du0 * sg).astype(BF16)
            db_ref[t:t + tr, :] = (du0 * av * sg * (1.0 - sg)).astype(BF16)
        for k in range(CONV_WIDTH):
            dw_ref[k:k + 1, :] = jnp.sum(dw_acc[k], axis=0, keepdims=True)
        dbias_ref[...] = jnp.sum(dbias_acc, axis=0, keepdims=True)

    col = lambda off: pl.BlockSpec((s, LANES), lambda c: (0, off + c))
    return pl.pallas_call(
        body, name=name, grid=(nblk,),
        in_specs=[col(0), col(0), col(a0), col(a0 + nblk),
                  pl.BlockSpec((CONV_WIDTH, LANES), lambda c: (0, c))],
        out_specs=[col(0), col(0), pl.BlockSpec((CONV_WIDTH, LANES), lambda c: (0, c)),
                   pl.BlockSpec((1, LANES), lambda c: (0, c))],
        out_shape=[jax.ShapeDtypeStruct((s, ch), BF16)] * 2
        + [jax.ShapeDtypeStruct((CONV_WIDTH, ch), F32), jax.ShapeDtypeStruct((1, ch), F32)],
        scratch_shapes=[pltpu.VMEM((s + 2 * CONV_PAD, LANES), F32)] * 2, compiler_params=_params(),
    )(duc, u0, proj, proj, conv_w)


GATE_BLK = 512


def _gate_fwd(proj, bg, y_a, y_b, col0, name):
    s, d = y_a.shape
    tm = 256
    g0 = col0 // GATE_BLK
    nb = d // GATE_BLK

    def body(ga_ref, gb_ref, ba_ref, bb_ref, ya_ref, yb_ref, o_ref):
        g_a = jax.nn.sigmoid(ga_ref[...] + ba_ref[...])
        g_b = jax.nn.sigmoid(gb_ref[...] + bb_ref[...])
        o_ref[...] = (g_a * ya_ref[...] + g_b * yb_ref[...]).astype(BF16)

    act = pl.BlockSpec((tm, GATE_BLK), lambda i, j: (i, j))
    return pl.pallas_call(
        body, name=name, grid=(s // tm, nb),
        in_specs=[pl.BlockSpec((tm, GATE_BLK), lambda i, j: (i, g0 + j)),
                  pl.BlockSpec((tm, GATE_BLK), lambda i, j: (i, g0 + nb + j)),
                  pl.BlockSpec((None, 1, GATE_BLK), lambda i, j: (0, 0, j)),
                  pl.BlockSpec((None, 1, GATE_BLK), lambda i, j: (1, 0, j)), act, act],
        out_specs=act, out_shape=jax.ShapeDtypeStruct((s, d), BF16), compiler_params=_params(),
    )(proj, proj, bg, bg, y_a, y_b)


def _gate_bwd(d_mixed, proj, bg, y_a, y_b, col0, name):
    s, d = y_a.shape
    tm = 256
    g0 = col0 // GATE_BLK
    nb = d // GATE_BLK

    def body(dm_ref, gl_ref, b_ref, y_ref, dgl_ref, dy_ref, db_ref):
        gate = jax.nn.sigmoid(gl_ref[...] + b_ref[...])
        dm = dm_ref[...]
        dy_ref[...] = (dm * gate).astype(BF16)
        dgl = dm * y_ref[...] * gate * (1.0 - gate)
        dgl_ref[...] = dgl.astype(BF16)
        part = jnp.sum(dgl, axis=0, keepdims=True)
        first = pl.program_id(1) == 0

        @pl.when(first)
        def _():
            db_ref[...] = part

        @pl.when(jnp.logical_not(first))
        def _():
            db_ref[...] += part

    def call(y, branch, dproj_prev):
        def wrapped(*refs):
            if dproj_prev is not None:
                refs = refs[1:]
            body(*refs)

        ins = [pl.BlockSpec((tm, GATE_BLK), lambda j, i: (i, j)),
               pl.BlockSpec((tm, GATE_BLK), lambda j, i: (i, g0 + branch * nb + j)),
               pl.BlockSpec((None, 1, GATE_BLK), lambda j, i: (branch, 0, j)),
               pl.BlockSpec((tm, GATE_BLK), lambda j, i: (i, j))]
        ops = [d_mixed, proj, bg, y]
        alias = {}
        if dproj_prev is not None:
            ins = [ANY] + ins
            ops = [dproj_prev] + ops
            alias = {0: 0}
        return pl.pallas_call(
            wrapped, name=f"{name}_{branch}", grid=(nb, s // tm), in_specs=ins,
            out_specs=[pl.BlockSpec((tm, GATE_BLK), lambda j, i: (i, g0 + branch * nb + j)),
                       pl.BlockSpec((tm, GATE_BLK), lambda j, i: (i, j)),
                       pl.BlockSpec((1, GATE_BLK), lambda j, i: (0, j))],
            out_shape=[jax.ShapeDtypeStruct((s, proj.shape[1]), BF16), jax.ShapeDtypeStruct((s, d), BF16),
                       jax.ShapeDtypeStruct((1, d), F32)],
            input_output_aliases=alias, compiler_params=_params(),
        )(*ops)

    dproj, dy_a, db_a = call(y_a, 0, None)
    dproj, dy_b, db_b = call(y_b, 1, dproj)
    return dproj, dy_a, dy_b, db_a, db_b


def _swiglu_fwd(gu, name):
    s, w2 = gu.shape
    ff = w2 // 2
    tm = 256
    tn = ff // 2

    def body(g_ref, u_ref, o_ref):
        gt = g_ref[...]
        o_ref[...] = (gt * jax.nn.sigmoid(gt) * u_ref[...]).astype(BF16)

    return pl.pallas_call(
        body, name=name, grid=(s // tm, 2),
        in_specs=[pl.BlockSpec((tm, tn), lambda i, j: (i, j)), pl.BlockSpec((tm, tn), lambda i, j: (i, 2 + j))],
        out_specs=pl.BlockSpec((tm, tn), lambda i, j: (i, j)),
        out_shape=jax.ShapeDtypeStruct((s, ff), BF16), compiler_params=_params(),
    )(gu, gu)


def _swiglu_bwd(gu, d_act, name):
    s, w2 = gu.shape
    ff = w2 // 2
    tm = 256
    tn = ff // 2

    def body(g_ref, u_ref, d_ref, o_ref):
        gt = g_ref[...]
        sg = jax.nn.sigmoid(gt)
        dv = d_ref[...]
        is_gate = pl.program_id(1) < 2

        @pl.when(is_gate)
        def _():
            o_ref[...] = (dv * u_ref[...] * (sg * (1.0 + gt * (1.0 - sg)))).astype(BF16)

        @pl.when(jnp.logical_not(is_gate))
        def _():
            o_ref[...] = (dv * gt * sg).astype(BF16)

    return pl.pallas_call(
        body, name=name, grid=(s // tm, 4),
        in_specs=[pl.BlockSpec((tm, tn), lambda i, j: (i, j % 2)),
                  pl.BlockSpec((tm, tn), lambda i, j: (i, 2 + j % 2)),
                  pl.BlockSpec((tm, tn), lambda i, j: (i, j % 2))],
        out_specs=pl.BlockSpec((tm, tn), lambda i, j: (i, j)),
        out_shape=jax.ShapeDtypeStruct((s, w2), BF16), compiler_params=_params(),
    )(gu, gu, d_act)


def _loss_fwd_bwd(y, target, name):
    s, d = y.shape
    tm = 256

    def body(y_ref, t_ref, dy_ref, dyb_ref, loss_ref, acc):
        diff = y_ref[...] - t_ref[...]
        dy = diff * (1.0 / d)
        dy_ref[...] = dy
        dyb_ref[...] = dy.astype(BF16)
        part = jnp.sum((diff * diff).reshape(tm // 8, 8, d), axis=0)
        i = pl.program_id(0)

        @pl.when(i == 0)
        def _():
            acc[...] = part

        @pl.when(i > 0)
        def _():
            acc[...] += part

        @pl.when(i == pl.num_programs(0) - 1)
        def _():
            loss_ref[...] = (0.5 / d) * jnp.sum(jnp.sum(acc[...], axis=1, keepdims=True), axis=0, keepdims=True)

    row = pl.BlockSpec((tm, d), lambda i: (i, 0))
    return pl.pallas_call(
        body, name=name, grid=(s // tm,), in_specs=[row, row],
        out_specs=[row, row, pl.BlockSpec((1, 1), lambda i: (0, 0))],
        out_shape=[jax.ShapeDtypeStruct((s, d), F32), jax.ShapeDtypeStruct((s, d), BF16),
                   jax.ShapeDtypeStruct((1, 1), F32)],
        scratch_shapes=[pltpu.VMEM((8, d), F32)], compiler_params=_params(),
    )(y, target)


def _local_step(x, pos_col, target, wts):
    consts = _rope_consts()
    bd = consts[3]
    qw2 = jnp.tile(wts["q_norm_w"], (1, LANES // HEAD_DIM))
    kw2 = jnp.tile(wts["k_norm_w"], (1, LANES // HEAD_DIM))
    qkv_w = 3 * N_SLOT_HEADS * HEAD_DIM
    conv_col0 = 3 * qkv_w
    ch = wts["conv_w"].shape[1]
    gate_col0 = conv_col0 + 2 * ch

    h = _rmsnorm_fwd(x, wts["norm1_w"], "rms1_fwd")
    proj = _matmul(h, wts["w_in"], mode="nn", tm=1024, tn=640, tk=1024, out_dtype=F32, name="mm_proj",
                   b_blocked=True)
    qn, kn = _qk_fwd(proj, pos_col, qw2, kw2, consts, "qk_fwd")
    attn, lse = _attn_fwd(qn, kn, proj, "attn_fwd")
    attn_b = attn.astype(BF16)
    y_a = _matmul(attn_b, wts["w_o_attn"], mode="nn", tm=1024, tn=256, tk=512, out_dtype=F32, name="mm_ya",
                  b_blocked=True)
    u0, uc = _conv_fwd(proj, wts["conv_w"], wts["conv_b"], conv_col0, "conv_fwd")
    u3 = _ln_silu_fwd(uc, wts["conv_ln_w"], wts["conv_ln_b"], "ln_fwd")
    y_b = _matmul(u3, wts["w_pw_conv"], mode="nn", tm=1024, tn=256, tk=512, out_dtype=F32, name="mm_yb",
                  b_blocked=True)
    mixed = _gate_fwd(proj, wts["b_gate"], y_a, y_b, gate_col0, "gate_fwd")
    x1 = _matmul(mixed, wts["w_out"], mode="nn", tm=1024, tn=512, tk=1024, out_dtype=F32, name="mm_x1",
                 residual=x)
    h2 = _rmsnorm_fwd(x1, wts["norm2_w"], "rms2_fwd")
    gu = _matmul(h2, wts["w_ffn_in"], mode="nn", tm=512, tn=1408, tk=1024, out_dtype=F32, name="mm_gu",
                 b_blocked=True)
    act = _swiglu_fwd(gu, "swiglu_fwd")
    x2 = _matmul(act, wts["w_ffn_out"], mode="nn", tm=512, tn=512, tk=2816, out_dtype=F32, name="mm_x2",
                 residual=x1)
    dy, dy_b16, loss = _loss_fwd_bwd(x2, target, "loss")

    g = {}
    d_act = _matmul(dy_b16, wts["w_ffn_out"], mode="nt", tm=512, tn=1408, tk=1024, out_dtype=F32, name="mm_dact")
    g["w_ffn_out"] = _matmul(act, dy_b16, mode="tn", tm=1408, tn=512, tk=2048, out_dtype=F32, name="mm_dwffnout")
    dgu = _swiglu_bwd(gu, d_act, "swiglu_bwd")
    dh2 = _matmul(dgu, wts["w_ffn_in"], mode="nt", tm=512, tn=1024, tk=1408, out_dtype=F32, name="mm_dh2",
                  b_blocked=True)
    g["w_ffn_in"] = _matmul(h2, dgu, mode="tn", tm=512, tn=1408, tk=2048, out_dtype=F32, name="mm_dwffnin",
                            out_blocked=N_CHIPS)
    dx1, dx1_b16, g["norm2_w"] = _rmsnorm_bwd(dh2, x1, wts["norm2_w"], dy, "rms2_bwd")
    d_mixed = _matmul(dx1_b16, wts["w_out"], mode="nt", tm=1024, tn=512, tk=1024, out_dtype=F32, name="mm_dmixed")
    g["w_out"] = _matmul(mixed, dx1_b16, mode="tn", tm=512, tn=1024, tk=2048, out_dtype=F32, name="mm_dwout")
    dproj, dy_a, dy_b, db_a, db_b = _gate_bwd(d_mixed, proj, wts["b_gate"], y_a, y_b, gate_col0, "gate_bwd")
    g["b_gate"] = jnp.concatenate([db_a, db_b], axis=0)
    dattn = _matmul(dy_a, wts["w_o_attn"], mode="nt", tm=1024, tn=512, tk=256, out_dtype=F32, name="mm_dattn",
                    b_blocked=True)
    g["w_o_attn"] = _matmul(attn_b, dy_a, mode="tn", tm=512, tn=256, tk=2048, out_dtype=F32, name="mm_dwo",
                            out_blocked=N_CHIPS)
    du3 = _matmul(dy_b, wts["w_pw_conv"], mode="nt", tm=1024, tn=512, tk=256, out_dtype=F32, name="mm_du3",
                  b_blocked=True)
    g["w_pw_conv"] = _matmul(u3, dy_b, mode="tn", tm=512, tn=256, tk=2048, out_dtype=F32, name="mm_dwpw",
                             out_blocked=N_CHIPS)
    duc, g["conv_ln_w"], g["conv_ln_b"] = _ln_silu_bwd(du3, uc, wts["conv_ln_w"], wts["conv_ln_b"], "ln_bwd")
    da, db, g["conv_w"], g["conv_b"] = _conv_bwd(duc, u0, proj, wts["conv_w"], conv_col0, "conv_bwd")
    dqn, dkn, dv = _attn_bwd(qn, kn, proj, dattn, attn, lse, bd, "attn_bwd")
    dproj, dqw, dkw = _qk_bwd(dproj, dqn, dkn, dv, da, db, proj, pos_col, qw2, kw2, consts, "qk_bwd")
    g["q_norm_w"] = dqw[:, :HEAD_DIM]
    g["k_norm_w"] = dkw[:, :HEAD_DIM]
    g["w_in"] = _matmul(h, dproj, mode="tn", tm=512, tn=640, tk=2048, out_dtype=F32, name="mm_dwin",
                        out_blocked=N_CHIPS)
    dh = _matmul(dproj, wts["w_in"], mode="nt", tm=512, tn=1024, tk=1920, out_dtype=F32, name="mm_dh",
                 b_blocked=True)
    grad_x, _, g["norm1_w"] = _rmsnorm_bwd(dh, x, wts["norm1_w"], dx1, "rms1_bwd")
    return loss, grad_x, g


def _mesh_pos():
    return lax.axis_index("x"), lax.axis_index("y"), lax.axis_index("c")


def _other_chips(x, y):
    return [(1 - x, y), (x, 1 - y), (1 - x, 1 - y)]


def _cast_into_slot(shard, chip_arr, dtype, name):
    r, c = shard.shape
    tr = r // 2 if r % 32 == 0 else r

    def body(chip_ref, s_ref, o_ref):
        del chip_ref
        o_ref[...] = s_ref[...].astype(dtype)

    return pl.pallas_call(
        body, name=name,
        grid_spec=pltpu.PrefetchScalarGridSpec(
            num_scalar_prefetch=1, grid=(r // tr,),
            in_specs=[pl.BlockSpec((tr, c), lambda i, chip_ref: (i, 0))],
            out_specs=pl.BlockSpec((None, tr, c), lambda i, chip_ref: (chip_ref[0], i, 0))),
        out_shape=jax.ShapeDtypeStruct((N_CHIPS, r, c), dtype), compiler_params=_params(),
    )(chip_arr, shard)


def _allgather_inplace(big, small, name):
    nb, ns = len(big), len(small)
    n = nb + ns

    def body(*refs):
        bufs = refs[n:2 * n]
        send_sems, recv_sems, fsend_sems, frecv_sems = refs[2 * n:]
        x, y, c = _mesh_pos()
        me = 2 * x + y
        chips = _other_chips(x, y)

        def part(a, slot, half):
            return bufs[a].at[slot, half] if a < nb else bufs[a].at[slot]

        sends = []
        for a in range(n):
            for k, (px, py) in enumerate(chips):
                cp = pltpu.make_async_remote_copy(
                    src_ref=part(a, me, c), dst_ref=part(a, me, c), send_sem=send_sems.at[a, k],
                    recv_sem=recv_sems.at[a, k], device_id=(px, py, c), device_id_type=MESH)
                cp.start()
                sends.append(cp)
        for a in range(n):
            for k, (px, py) in enumerate(chips):
                slot = 2 * px + py
                pltpu.make_async_remote_copy(
                    src_ref=part(a, slot, c), dst_ref=part(a, slot, c), send_sem=send_sems.at[a, k],
                    recv_sem=recv_sems.at[a, k], device_id=(px, py, c), device_id_type=MESH).wait_recv()
                if a < nb:
                    fwd = pltpu.make_async_remote_copy(
                        src_ref=part(a, slot, c), dst_ref=part(a, slot, c), send_sem=fsend_sems.at[a, k],
                        recv_sem=frecv_sems.at[a, k], device_id=(x, y, 1 - c), device_id_type=MESH)
                    fwd.start()
                    sends.append(fwd)
        for a in range(nb):
            for k, (px, py) in enumerate(chips):
                slot = 2 * px + py
                pltpu.make_async_remote_copy(
                    src_ref=part(a, slot, 1 - c), dst_ref=part(a, slot, 1 - c), send_sem=fsend_sems.at[a, k],
                    recv_sem=frecv_sems.at[a, k], device_id=(x, y, 1 - c), device_id_type=MESH).wait_recv()
        for cp in sends:
            cp.wait_send()

    ops = list(big) + list(small)
    return pl.pallas_call(
        body, name=name, in_specs=[ANY] * n, out_specs=[ANY] * n,
        out_shape=[jax.ShapeDtypeStruct(o.shape, o.dtype) for o in ops],
        input_output_aliases={i: i for i in range(n)},
        scratch_shapes=[pltpu.SemaphoreType.DMA((n, 3)), pltpu.SemaphoreType.DMA((n, 3)),
                        pltpu.SemaphoreType.DMA((nb, 3)), pltpu.SemaphoreType.DMA((nb, 3))],
    )(*ops)


def _pair_exchange(gs, name):
    n = len(gs)

    def body(*refs):
        ins, outs = refs[:n], refs[n:2 * n]
        send_sems, recv_sems = refs[2 * n:]
        x, y, c = _mesh_pos()
        copies = []
        for a in range(n):
            cp = pltpu.make_async_remote_copy(
                src_ref=ins[a].at[:, 1 - c], dst_ref=outs[a], send_sem=send_sems.at[a],
                recv_sem=recv_sems.at[a], device_id=(x, y, 1 - c), device_id_type=MESH)
            cp.start()
            copies.append(cp)
        for cp in copies:
            cp.wait()

    return pl.pallas_call(
        body, name=name, in_specs=[ANY] * n, out_specs=[ANY] * n,
        out_shape=[jax.ShapeDtypeStruct((g.shape[0],) + g.shape[2:], g.dtype) for g in gs],
        scratch_shapes=[pltpu.SemaphoreType.DMA((n,)), pltpu.SemaphoreType.DMA((n,))],
    )(*gs)


def _add_own_half(g, recv, pos_arr, name):
    nb, _, rh, cols = g.shape

    def body(pos_ref, g_ref, r_ref, send_ref, own_ref):
        s = (g_ref[...] + r_ref[...]).astype(BF16)
        send_ref[...] = s

        @pl.when(pl.program_id(0) == pos_ref[1])
        def _():
            own_ref[...] = s

    blk = pl.BlockSpec((None, rh, cols), lambda j, pos_ref: (j, 0, 0))
    shape = jax.ShapeDtypeStruct((nb, rh, cols), BF16)
    return pl.pallas_call(
        body, name=name,
        grid_spec=pltpu.PrefetchScalarGridSpec(
            num_scalar_prefetch=1, grid=(nb,),
            in_specs=[pl.BlockSpec((None, None, rh, cols), lambda j, pos_ref: (j, pos_ref[0], 0, 0)), blk],
            out_specs=[blk, pl.BlockSpec((None, rh, cols), lambda j, pos_ref: (pos_ref[1], 0, 0))]),
        out_shape=[shape, shape], compiler_params=_params(),
    )(pos_arr, g, recv)


def _chip_exchange(to_send, by_chip, name):
    n = len(to_send)

    def body(*refs):
        ins, bufs = refs[:n], refs[2 * n:3 * n]
        send_sems, recv_sems = refs[3 * n:]
        x, y, c = _mesh_pos()
        me = 2 * x + y
        chips = _other_chips(x, y)
        sends = []
        for a in range(n):
            for k, (px, py) in enumerate(chips):
                cp = pltpu.make_async_remote_copy(
                    src_ref=ins[a].at[2 * px + py], dst_ref=bufs[a].at[me], send_sem=send_sems.at[a, k],
                    recv_sem=recv_sems.at[a, k], device_id=(px, py, c), device_id_type=MESH)
                cp.start()
                sends.append(cp)
        for a in range(n):
            for k, (px, py) in enumerate(chips):
                pltpu.make_async_remote_copy(
                    src_ref=ins[a].at[me], dst_ref=bufs[a].at[2 * px + py], send_sem=send_sems.at[a, k],
                    recv_sem=recv_sems.at[a, k], device_id=(px, py, c), device_id_type=MESH).wait_recv()
        for cp in sends:
            cp.wait_send()

    return pl.pallas_call(
        body, name=name, in_specs=[ANY] * (2 * n), out_specs=[ANY] * n,
        out_shape=[jax.ShapeDtypeStruct(b.shape, b.dtype) for b in by_chip],
        input_output_aliases={n + i: i for i in range(n)},
        scratch_shapes=[pltpu.SemaphoreType.DMA((n, 3)), pltpu.SemaphoreType.DMA((n, 3))],
    )(*to_send, *by_chip)


def _sum_chips(gath, pos_arr, name):
    nb, rh, cols = gath.shape

    def body(pos_ref, a_ref, b_ref, c_ref, d_ref, o_ref):
        del pos_ref
        o_ref[...] = ((a_ref[...].astype(F32) + b_ref[...].astype(F32)) + c_ref[...].astype(F32)) \
            + d_ref[...].astype(F32)

    tr = rh // 2 if (rh // 2) % 16 == 0 else rh
    specs = [pl.BlockSpec((None, tr, cols), functools.partial(lambda i, pos_ref, j: (j, i, 0), j=j))
             for j in range(nb)]
    return pl.pallas_call(
        body, name=name,
        grid_spec=pltpu.PrefetchScalarGridSpec(
            num_scalar_prefetch=1, grid=(rh // tr,), in_specs=specs,
            out_specs=pl.BlockSpec((None, tr, cols), lambda i, pos_ref: (pos_ref[0], i, 0))),
        out_shape=jax.ShapeDtypeStruct((2, rh, cols), F32), compiler_params=_params(),
    )(pos_arr, gath, gath, gath, gath)


def _pair_gather(bufs, name):
    n = len(bufs)

    def body(*refs):
        outs = refs[n:2 * n]
        send_sems, recv_sems = refs[2 * n:]
        x, y, c = _mesh_pos()
        sends = []
        for a in range(n):
            cp = pltpu.make_async_remote_copy(
                src_ref=outs[a].at[c], dst_ref=outs[a].at[c], send_sem=send_sems.at[a],
                recv_sem=recv_sems.at[a], device_id=(x, y, 1 - c), device_id_type=MESH)
            cp.start()
            sends.append(cp)
        for a in range(n):
            pltpu.make_async_remote_copy(
                src_ref=outs[a].at[1 - c], dst_ref=outs[a].at[1 - c], send_sem=send_sems.at[a],
                recv_sem=recv_sems.at[a], device_id=(x, y, 1 - c), device_id_type=MESH).wait_recv()
        for cp in sends:
            cp.wait_send()

    return pl.pallas_call(
        body, name=name, in_specs=[ANY] * n, out_specs=[ANY] * n,
        out_shape=[jax.ShapeDtypeStruct(b.shape, b.dtype) for b in bufs],
        input_output_aliases={i: i for i in range(n)},
        scratch_shapes=[pltpu.SemaphoreType.DMA((n,)), pltpu.SemaphoreType.DMA((n,))],
    )(*bufs)


def _small_allreduce(v, name):
    n = v.shape[0]
    n_dev = 8

    def body(v_ref, o_ref, buf, send_sems, recv_sems):
        x, y, c = _mesh_pos()
        me = 4 * x + 2 * y + c
        buf[me] = v_ref[...]
        sends = []
        peers = []
        for r in range(1, n_dev):
            px = 1 - x if r & 4 else x
            py = 1 - y if r & 2 else y
            pc = 1 - c if r & 1 else c
            peers.append((px, py, pc))
            cp = pltpu.make_async_remote_copy(
                src_ref=v_ref, dst_ref=buf.at[me], send_sem=send_sems.at[r - 1],
                recv_sem=recv_sems.at[r - 1], device_id=(px, py, pc), device_id_type=MESH)
            cp.start()
            sends.append(cp)
        for r, (px, py, pc) in enumerate(peers):
            pltpu.make_async_remote_copy(
                src_ref=v_ref, dst_ref=buf.at[4 * px + 2 * py + pc], send_sem=send_sems.at[r],
                recv_sem=recv_sems.at[r], device_id=(px, py, pc), device_id_type=MESH).wait_recv()
        for cp in sends:
            cp.wait_send()
        acc = buf[0]
        for i in range(1, n_dev):
            acc = acc + buf[i]
        o_ref[...] = acc

    vm = pl.BlockSpec(memory_space=pltpu.VMEM)
    return pl.pallas_call(
        body, name=name, in_specs=[vm], out_specs=vm, out_shape=jax.ShapeDtypeStruct(v.shape, v.dtype),
        scratch_shapes=[pltpu.VMEM((n_dev, n, LANES), F32), pltpu.SemaphoreType.DMA((n_dev - 1,)),
                        pltpu.SemaphoreType.DMA((n_dev - 1,))],
        compiler_params=_params(),
    )(v)


def _adamw_math(w, g, m, v):
    m = ADAM_B1 * m + (1.0 - ADAM_B1) * g
    v = ADAM_B2 * v + (1.0 - ADAM_B2) * (g * g)
    m_hat = m / (1.0 - ADAM_B1 ** ADAM_STEP)
    v_hat = v / (1.0 - ADAM_B2 ** ADAM_STEP)
    delta = -ADAM_LR * (m_hat / (jnp.sqrt(v_hat) + ADAM_EPS) + ADAM_WD * w)
    return delta, m, v


def _adamw(w, g, m, v, name):
    r, c = w.shape
    tr = 128 if r % 128 == 0 else 64
    assert r % tr == 0

    def body(w_ref, g_ref, m_ref, v_ref, d_ref, mo_ref, vo_ref):
        d, mn, vn = _adamw_math(w_ref[...], g_ref[...], m_ref[...], v_ref[...])
        d_ref[...] = d
        mo_ref[...] = mn
        vo_ref[...] = vn

    blk = pl.BlockSpec((tr, c), lambda i: (i, 0))
    return pl.pallas_call(
        body, name=name, grid=(r // tr,), in_specs=[blk] * 4, out_specs=[blk] * 3,
        out_shape=[jax.ShapeDtypeStruct((r, c), F32)] * 3, compiler_params=_params(),
    )(w, g, m, v)


def _adamw_small(ws, gs, ms, vs, name):
    n = len(ws)

    def body(*refs):
        w_r, g_r, m_r, v_r = refs[:n], refs[n:2 * n], refs[2 * n:3 * n], refs[3 * n:4 * n]
        d_o, m_o, v_o = refs[4 * n:5 * n], refs[5 * n:6 * n], refs[6 * n:7 * n]
        for i in range(n):
            d, mn, vn = _adamw_math(w_r[i][...], g_r[i][...], m_r[i][...], v_r[i][...])
            d_o[i][...] = d
            m_o[i][...] = mn
            v_o[i][...] = vn

    vm = pl.BlockSpec(memory_space=pltpu.VMEM)
    shapes = [jax.ShapeDtypeStruct(w.shape, F32) for w in ws]
    outs = pl.pallas_call(
        body, name=name, in_specs=[vm] * (4 * n), out_specs=[vm] * (3 * n), out_shape=shapes * 3,
        compiler_params=_params(),
    )(*ws, *gs, *ms, *vs)
    return outs[:n], outs[n:2 * n], outs[2 * n:]


BIG = ("w_in", "w_o_attn", "w_pw_conv", "w_out", "w_ffn_in", "w_ffn_out")
ROW_SHARDED = ("w_out", "w_ffn_out")
SMALL = ("norm1_w", "b_gate", "q_norm_w", "k_norm_w", "conv_w", "conv_b", "conv_ln_w", "conv_ln_b", "norm2_w")
ORDER = ("norm1_w", "w_in", "b_gate", "q_norm_w", "k_norm_w", "w_o_attn", "conv_w", "conv_b", "conv_ln_w",
         "conv_ln_b", "w_pw_conv", "w_out", "norm2_w", "w_ffn_in", "w_ffn_out")
PACK_TILE = 8 * LANES


def _pack_small(parts):
    rows = []
    for p in parts:
        flat = p.reshape(-1)
        pad = (-flat.shape[0]) % PACK_TILE
        rows.append(jnp.pad(flat, (0, pad)).reshape(-1, LANES))
    return jnp.concatenate(rows, axis=0)


def _unpack_small(packed, shapes):
    out, row = [], 0
    for shp in shapes:
        size = int(np.prod(shp))
        nrow = -(-size // PACK_TILE) * (PACK_TILE // LANES)
        out.append(packed[row:row + nrow].reshape(-1)[:size].reshape(shp))
        row += nrow
    return out


def kernel(x, positions, norm1_w, w_in, b_gate, q_norm_w, k_norm_w, w_o_attn, conv_w, conv_b, conv_ln_w, conv_ln_b, w_pw_conv, w_out, norm2_w, w_ffn_in, w_ffn_out, loss_target, m_norm1_w, m_w_in, m_b_gate, m_q_norm_w, m_k_norm_w, m_w_o_attn, m_conv_w, m_conv_b, m_conv_ln_w, m_conv_ln_b, m_w_pw_conv, m_w_out, m_norm2_w, m_w_ffn_in, m_w_ffn_out, v_norm1_w, v_w_in, v_b_gate, v_q_norm_w, v_k_norm_w, v_w_o_attn, v_conv_w, v_conv_b, v_conv_ln_w, v_conv_ln_b, v_w_pw_conv, v_w_out, v_norm2_w, v_w_ffn_in, v_w_ffn_out):
    w = dict(norm1_w=norm1_w, w_in=w_in, b_gate=b_gate, q_norm_w=q_norm_w, k_norm_w=k_norm_w, w_o_attn=w_o_attn,
             conv_w=conv_w, conv_b=conv_b, conv_ln_w=conv_ln_w, conv_ln_b=conv_ln_b, w_pw_conv=w_pw_conv,
             w_out=w_out, norm2_w=norm2_w, w_ffn_in=w_ffn_in, w_ffn_out=w_ffn_out)
    m = dict(norm1_w=m_norm1_w, w_in=m_w_in, b_gate=m_b_gate, q_norm_w=m_q_norm_w, k_norm_w=m_k_norm_w,
             w_o_attn=m_w_o_attn, conv_w=m_conv_w, conv_b=m_conv_b, conv_ln_w=m_conv_ln_w,
             conv_ln_b=m_conv_ln_b, w_pw_conv=m_w_pw_conv, w_out=m_w_out, norm2_w=m_norm2_w,
             w_ffn_in=m_w_ffn_in, w_ffn_out=m_w_ffn_out)
    v = dict(norm1_w=v_norm1_w, w_in=v_w_in, b_gate=v_b_gate, q_norm_w=v_q_norm_w, k_norm_w=v_k_norm_w,
             w_o_attn=v_w_o_attn, conv_w=v_conv_w, conv_b=v_conv_b, conv_ln_w=v_conv_ln_w,
             conv_ln_b=v_conv_ln_b, w_pw_conv=v_w_pw_conv, w_out=v_w_out, norm2_w=v_norm2_w,
             w_ffn_in=v_w_ffn_in, w_ffn_out=v_w_ffn_out)
    cx, cy, cc = _mesh_pos()
    chip = 2 * cx + cy

    chip_arr = chip.reshape(1).astype(jnp.int32)
    pos_arr = jnp.stack([cc, chip]).astype(jnp.int32)
    big_bufs = []
    for n in BIG:
        buf = _cast_into_slot(w[n][0], chip_arr, BF16, f"cast_{n}")
        big_bufs.append(buf.reshape(N_CHIPS, 2, buf.shape[1] // 2, buf.shape[2]))
    small_bufs = [_cast_into_slot(w[n][0], chip_arr, F32, f"slot_{n}") for n in ("conv_w", "b_gate")]
    gathered = _allgather_inplace(big_bufs, small_bufs, "allgather_weights")
    full = {n: gb.reshape(N_CHIPS, -1, gb.shape[3]) for n, gb in zip(BIG, gathered[:len(BIG)])}
    for n in ROW_SHARDED:
        full[n] = full[n].reshape(-1, full[n].shape[2])
    conv_w_full = gathered[len(BIG)].transpose(1, 0, 2).reshape(CONV_WIDTH, -1)
    b_gate_full = gathered[len(BIG) + 1].transpose(1, 0, 2).reshape(2, 1, -1)
    wts = dict(full, conv_w=conv_w_full, b_gate=b_gate_full, norm1_w=norm1_w, q_norm_w=q_norm_w,
               k_norm_w=k_norm_w, conv_b=conv_b, conv_ln_w=conv_ln_w, conv_ln_b=conv_ln_b, norm2_w=norm2_w)

    loss, grad_x, g = _local_step(x[0], positions.reshape(-1, 1), loss_target[0], wts)

    g_big = []
    for n in BIG:
        gb = g[n] if g[n].ndim == 3 else g[n].reshape(N_CHIPS, g[n].shape[0] // N_CHIPS, g[n].shape[1])
        g_big.append(gb.reshape(N_CHIPS, 2, gb.shape[1] // 2, gb.shape[2]))
    from_sibling = _pair_exchange(g_big, "grads_pair_exchange")
    pair_sums = [_add_own_half(gb, rv, pos_arr, f"grads_pair_add_{n}") for n, gb, rv in zip(BIG, g_big, from_sibling)]
    by_chip = _chip_exchange([p[0] for p in pair_sums], [p[1] for p in pair_sums], "grads_chip_exchange")
    halves = [_sum_chips(bc, pos_arr, f"grads_chip_sum_{n}") for n, bc in zip(BIG, by_chip)]
    both = _pair_gather(halves, "grads_pair_gather")
    grads = {n: b.reshape(-1, b.shape[2]) for n, b in zip(BIG, both)}

    small_parts = [loss] + [g[n] for n in SMALL]
    small_shapes = [p.shape for p in small_parts]
    reduced = _unpack_small(_small_allreduce(_pack_small(small_parts), "small_allreduce"), small_shapes)
    loss_total = reduced[0].reshape(())
    for n, r in zip(SMALL, reduced[1:]):
        grads[n] = r
    ch_shard = conv_w.shape[2]
    grads["conv_w"] = lax.dynamic_slice_in_dim(grads["conv_w"], chip * ch_shard, ch_shard, axis=1)
    d_shard = b_gate.shape[2]
    grads["b_gate"] = lax.dynamic_slice_in_dim(grads["b_gate"], chip * d_shard, d_shard, axis=1)

    delta, new_m, new_v = {}, {}, {}
    for n in BIG:
        delta[n], new_m[n], new_v[n] = _adamw(w[n][0], grads[n], m[n][0], v[n][0], f"adamw_{n}")
    flat2 = lambda a: a.reshape(-1, a.shape[-1])
    d_s, m_s, v_s = _adamw_small([flat2(w[n]) for n in SMALL], [flat2(grads[n]) for n in SMALL],
                                 [flat2(m[n]) for n in SMALL], [flat2(v[n]) for n in SMALL], "adamw_small")
    for i, n in enumerate(SMALL):
        delta[n], new_m[n], new_v[n] = d_s[i], m_s[i], v_s[i]

    shaped = lambda d, n: d[n].reshape(w[n].shape)
    return (loss_total, grad_x[None], *[shaped(grads, n) for n in ORDER], *[shaped(delta, n) for n in ORDER],
            *[shaped(new_m, n) for n in ORDER], *[shaped(new_v, n) for n in ORDER])
```

```python
import functools

import numpy as np
import jax
import jax.numpy as jnp
from jax import lax
from jax.experimental import pallas as pl
from jax.experimental.pallas import tpu as pltpu

F32 = jnp.float32
BF16 = jnp.bfloat16
MESH = pl.DeviceIdType.MESH
ANY = pl.BlockSpec(memory_space=pl.ANY)

HEAD_DIM = 64
N_SLOT_HEADS = 8
DILATIONS = (1, 4, 16)
HALF_SPAN = 64
ROPE_THETA = 500000.0
ROT_DIM = 16
CONV_WIDTH = 31
EPS = 1e-6
NEG_INF = -1e30
ADAM_LR, ADAM_B1, ADAM_B2, ADAM_EPS, ADAM_WD, ADAM_STEP = 0.001, 0.9, 0.999, 1e-08, 0.01, 10

LANES = 128
QBLK = 128
KWIN = QBLK + 2 * HALF_SPAN
VMEM_LIMIT = 48 * 1024 * 1024
N_CHIPS = 4
HIGHEST = lax.Precision.HIGHEST


def _params(**kw):
    return pltpu.CompilerParams(vmem_limit_bytes=VMEM_LIMIT, **kw)


class _Rider:
    def __init__(self, operands, out_shapes, aliases, scratch, start, wait):
        self.operands, self.out_shapes, self.aliases = list(operands), list(out_shapes), dict(aliases)
        self.scratch, self.start, self.wait = list(scratch), start, wait


def _pallas(body, *, name, grid, in_specs, out_specs, out_shape, operands, scratch_shapes=(), aliases=None,
            rider=None):
    single = not isinstance(out_specs, (list, tuple))
    out_specs_l = [out_specs] if single else list(out_specs)
    out_shape_l = [out_shape] if single else list(out_shape)
    aliases = dict(aliases or {})
    if rider is None:
        res = pl.pallas_call(
            body, name=name, grid=grid, in_specs=list(in_specs), out_specs=out_specs_l, out_shape=out_shape_l,
            scratch_shapes=list(scratch_shapes), input_output_aliases=aliases, compiler_params=_params(),
        )(*operands)
        return res[0] if single else res
    n_in, n_rin = len(in_specs), len(rider.operands)
    n_out, n_rout = len(out_specs_l), len(rider.out_shapes)
    n_sc = len(scratch_shapes)

    def wrapped(*refs):
        main_in, r_in = refs[:n_in], refs[n_in:n_in + n_rin]
        o0 = n_in + n_rin
        main_out, r_out = refs[o0:o0 + n_out], refs[o0 + n_out:o0 + n_out + n_rout]
        s0 = o0 + n_out + n_rout
        main_sc, r_sc = refs[s0:s0 + n_sc], refs[s0 + n_sc:]
        ids = [pl.program_id(d) for d in range(len(grid))]
        first = functools.reduce(jnp.logical_and, [i == 0 for i in ids])
        last = functools.reduce(jnp.logical_and, [i == n - 1 for i, n in zip(ids, grid)])

        @pl.when(first)
        def _():
            rider.start(r_in, r_out, r_sc)

        body(*main_in, *main_out, *main_sc)

        @pl.when(last)
        def _():
            rider.wait(r_in, r_out, r_sc)

    for src, dst in rider.aliases.items():
        aliases[n_in + src] = n_out + dst
    res = pl.pallas_call(
        wrapped, name=name, grid=grid, in_specs=list(in_specs) + [ANY] * n_rin,
        out_specs=out_specs_l + [ANY] * n_rout, out_shape=out_shape_l + rider.out_shapes,
        scratch_shapes=list(scratch_shapes) + rider.scratch, input_output_aliases=aliases,
        compiler_params=_params(),
    )(*operands, *rider.operands)
    main = res[:n_out]
    return (main[0] if single else main), res[n_out:]


def _matmul(a, b, *, mode, tm, tn, tk, out_dtype, name, b_blocked=False,
            out_blocked=None, residual=None, rider=None):
    a_shape = a.shape
    if mode == "nn":
        m_dim, k_dim = a_shape
        n_dim = b.shape[0] * b.shape[2] if b_blocked else b.shape[1]
        rows, cols, red = m_dim, n_dim, k_dim
    elif mode == "nt":
        m_dim, n_dim = a_shape
        k_dim = b.shape[1] if b_blocked else b.shape[0]
        rows, cols, red = m_dim, k_dim, n_dim
    else:
        m_dim, k_dim = a_shape
        n_dim = b.shape[1]
        rows, cols, red = k_dim, n_dim, m_dim
    assert rows % tm == 0 and cols % tn == 0 and red % tk == 0, (name, rows, cols, red)
    ni, nj, nk = rows // tm, cols // tn, red // tk

    if mode == "nn":
        a_spec = pl.BlockSpec((tm, tk), lambda i, j, k: (i, k))
        if b_blocked:
            per = b.shape[2] // tn
            b_spec = pl.BlockSpec((None, tk, tn), lambda i, j, k: (j // per, k, j % per))
        else:
            b_spec = pl.BlockSpec((tk, tn), lambda i, j, k: (k, j))
        dims = (((1,), (0,)), ((), ()))
    elif mode == "nt":
        a_spec = pl.BlockSpec((tm, tk), lambda i, j, k: (i, k))
        if b_blocked:
            per = b.shape[2] // tk
            b_spec = pl.BlockSpec((None, tn, tk), lambda i, j, k: (k // per, j, k % per))
        else:
            b_spec = pl.BlockSpec((tn, tk), lambda i, j, k: (j, k))
        dims = (((1,), (1,)), ((), ()))
    else:
        a_spec = pl.BlockSpec((tk, tm), lambda i, j, k: (k, i))
        b_spec = pl.BlockSpec((tk, tn), lambda i, j, k: (k, j))
        dims = (((0,), (0,)), ((), ()))

    if out_blocked:
        per_o = (cols // out_blocked) // tn
        out_spec = pl.BlockSpec((None, tm, tn), lambda i, j, k: (j // per_o, i, j % per_o))
        out_shape = jax.ShapeDtypeStruct((out_blocked, rows, cols // out_blocked), out_dtype)
    else:
        out_spec = pl.BlockSpec((tm, tn), lambda i, j, k: (i, j))
        out_shape = jax.ShapeDtypeStruct((rows, cols), out_dtype)

    in_specs = [a_spec, b_spec]
    operands = [a, b]
    if residual is not None:
        in_specs.append(pl.BlockSpec((tm, tn), lambda i, j, k: (i, j)))
        operands.append(residual)
    has_res = residual is not None

    def body(*refs):
        a_ref, b_ref = refs[0], refs[1]
        res_ref = refs[2] if has_res else None
        o_ref = refs[3] if has_res else refs[2]
        prod = lax.dot_general(a_ref[...], b_ref[...], dims, preferred_element_type=F32)

        def finish(val):
            if has_res:
                val = val + res_ref[...]
            o_ref[...] = val.astype(out_dtype)

        if nk == 1:
            finish(prod)
        else:
            acc_ref = refs[-1]
            k = pl.program_id(2)

            @pl.when(k == 0)
            def _():
                acc_ref[...] = prod

            @pl.when(k > 0)
            def _():
                acc_ref[...] += prod

            @pl.when(k == nk - 1)
            def _():
                finish(acc_ref[...])

    scratch = [pltpu.VMEM((tm, tn), F32)] if nk > 1 else []
    return _pallas(body, name=name, grid=(ni, nj, nk), in_specs=in_specs, out_specs=out_spec,
                   out_shape=out_shape, operands=operands, scratch_shapes=scratch, rider=rider)


def _rmsnorm_fwd(x, w, name):
    s, d = x.shape
    tm = 256

    def body(x_ref, w_ref, o_ref):
        xv = x_ref[...]
        rstd = lax.rsqrt(jnp.mean(xv * xv, axis=-1, keepdims=True) + EPS)
        o_ref[...] = (xv * rstd * w_ref[...]).astype(BF16)

    return pl.pallas_call(
        body, name=name, grid=(s // tm,),
        in_specs=[pl.BlockSpec((tm, d), lambda i: (i, 0)), pl.BlockSpec((1, d), lambda i: (0, 0))],
        out_specs=pl.BlockSpec((tm, d), lambda i: (i, 0)),
        out_shape=jax.ShapeDtypeStruct((s, d), BF16), compiler_params=_params(),
    )(x, w)


def _rmsnorm_bwd(dh, x, w, dres, name):
    s, d = x.shape
    tm = 256

    def body(dh_ref, x_ref, w_ref, dres_ref, dx_ref, dxb_ref, dw_ref):
        xv = x_ref[...]
        rstd = lax.rsqrt(jnp.mean(xv * xv, axis=-1, keepdims=True) + EPS)
        xhat = xv * rstd
        dhv = dh_ref[...]
        g = dhv * w_ref[...]
        dx = rstd * (g - xhat * jnp.mean(g * xhat, axis=-1, keepdims=True)) + dres_ref[...]
        dx_ref[...] = dx
        dxb_ref[...] = dx.astype(BF16)
        part = jnp.sum(dhv * xhat, axis=0, keepdims=True)

        @pl.when(pl.program_id(0) == 0)
        def _():
            dw_ref[...] = part

        @pl.when(pl.program_id(0) > 0)
        def _():
            dw_ref[...] += part

    row = pl.BlockSpec((tm, d), lambda i: (i, 0))
    vec = pl.BlockSpec((1, d), lambda i: (0, 0))
    return pl.pallas_call(
        body, name=name, grid=(s // tm,), in_specs=[row, row, vec, row], out_specs=[row, row, vec],
        out_shape=[jax.ShapeDtypeStruct((s, d), F32), jax.ShapeDtypeStruct((s, d), BF16),
                   jax.ShapeDtypeStruct((1, d), F32)],
        compiler_params=_params(),
    )(dh, x, w, dres)


def _rope_consts():
    lane = np.arange(LANES)
    in_head = lane % HEAD_DIM
    inv_freq = ROPE_THETA ** (-jnp.arange(0, ROT_DIM, 2, dtype=F32) / ROT_DIM)
    invf = jnp.where(jnp.asarray(in_head < ROT_DIM), jnp.tile(inv_freq, LANES // (ROT_DIM // 2)), 0.0)
    m_a = np.where(in_head < ROT_DIM // 2, -1.0, 0.0).astype(np.float32)
    m_b = np.where((in_head >= ROT_DIM // 2) & (in_head < ROT_DIM), 1.0, 0.0).astype(np.float32)
    block_diag = (lane[:, None] // HEAD_DIM == lane[None, :] // HEAD_DIM).astype(np.float32)
    return (invf.reshape(1, LANES).astype(F32), jnp.asarray(m_a).reshape(1, LANES),
            jnp.asarray(m_b).reshape(1, LANES), jnp.asarray(block_diag))


def _head_sums(v, bd):
    return jnp.dot(v, bd, precision=HIGHEST, preferred_element_type=F32)


def _qk_fwd(proj, pos_col, qw2, kw2, consts, name):
    s = proj.shape[0]
    width = 3 * N_SLOT_HEADS * HEAD_DIM
    tm = 256
    invf, m_a, m_b, bd = consts
    scale = HEAD_DIM ** -0.5

    def body(q_ref, k_ref, pos_ref, qw_ref, kw_ref, invf_ref, ma_ref, mb_ref, bd_ref, qo_ref, ko_ref):
        ang = pos_ref[...].astype(F32) * invf_ref[...]
        cos = jnp.cos(ang)
        sin = jnp.sin(ang)
        s_a = sin * ma_ref[...]
        s_b = sin * mb_ref[...]
        bdv = bd_ref[...]
        for src, w_ref, dst, sc in ((q_ref, qw_ref, qo_ref, scale), (k_ref, kw_ref, ko_ref, 1.0)):
            for cb in range(width // LANES):
                cols = slice(cb * LANES, (cb + 1) * LANES)
                t = src[:, cols]
                rstd = lax.rsqrt(_head_sums(t * t, bdv) * (1.0 / HEAD_DIM) + EPS)
                y = t * rstd * w_ref[...]
                r = y * cos + pltpu.roll(y, LANES - 8, axis=1) * s_a + pltpu.roll(y, 8, axis=1) * s_b
                dst[:, cols] = r * sc if sc != 1.0 else r

    vec = pl.BlockSpec((1, LANES), lambda i: (0, 0))
    return pl.pallas_call(
        body, name=name, grid=(s // tm,),
        in_specs=[pl.BlockSpec((tm, width), lambda i: (i, 0)), pl.BlockSpec((tm, width), lambda i: (i, 1)),
                  pl.BlockSpec((tm, 1), lambda i: (i, 0)), vec, vec, vec, vec, vec,
                  pl.BlockSpec((LANES, LANES), lambda i: (0, 0))],
        out_specs=[pl.BlockSpec((tm, width), lambda i: (i, 0))] * 2,
        out_shape=[jax.ShapeDtypeStruct((s, width), F32)] * 2, compiler_params=_params(),
    )(proj, proj, pos_col, qw2, kw2, invf, m_a, m_b, bd)


def _qk_bwd(dproj, dqn, dkn, dv, da, db, proj, pos_col, qw2, kw2, consts, name, rider=None):
    s = proj.shape[0]
    width = 3 * N_SLOT_HEADS * HEAD_DIM
    ch = da.shape[1]
    out_w = 3 * width + 2 * ch
    tm = 256
    invf, m_a, m_b, bd = consts
    scale = HEAD_DIM ** -0.5

    def body(dproj_in, dq_ref, dk_ref, dv_ref, da_ref, db_ref, q_ref, k_ref, pos_ref, qw_ref, kw_ref,
             invf_ref, ma_ref, mb_ref, bd_ref, out_ref, dqw_ref, dkw_ref):
        del dproj_in
        ang = pos_ref[...].astype(F32) * invf_ref[...]
        cos = jnp.cos(ang)
        sin = jnp.sin(ang)
        s_a = sin * ma_ref[...]
        s_b = sin * mb_ref[...]
        bdv = bd_ref[...]
        first = pl.program_id(0) == 0
        for src, dsrc, w_ref, col0, dw_ref, sc in ((q_ref, dq_ref, qw_ref, 0, dqw_ref, scale),
                                                   (k_ref, dk_ref, kw_ref, width, dkw_ref, 1.0)):
            dw_acc = jnp.zeros((1, LANES), F32)
            for cb in range(width // LANES):
                cols = slice(cb * LANES, (cb + 1) * LANES)
                t = src[:, cols]
                dr = dsrc[:, cols]
                if sc != 1.0:
                    dr = dr * sc
                dy = dr * cos + pltpu.roll(dr * s_a, 8, axis=1) + pltpu.roll(dr * s_b, LANES - 8, axis=1)
                rstd = lax.rsqrt(_head_sums(t * t, bdv) * (1.0 / HEAD_DIM) + EPS)
                xhat = t * rstd
                g = dy * w_ref[...]
                dt = rstd * (g - xhat * (_head_sums(g * xhat, bdv) * (1.0 / HEAD_DIM)))
                out_ref[:, col0 + cb * LANES: col0 + (cb + 1) * LANES] = dt.astype(BF16)
                dw_acc = dw_acc + jnp.sum(dy * xhat, axis=0, keepdims=True)
            dw_acc = dw_acc + pltpu.roll(dw_acc, HEAD_DIM, axis=1)

            @pl.when(first)
            def _(dw_ref=dw_ref, dw_acc=dw_acc):
                dw_ref[...] = dw_acc

            @pl.when(jnp.logical_not(first))
            def _(dw_ref=dw_ref, dw_acc=dw_acc):
                dw_ref[...] += dw_acc
        out_ref[:, 2 * width: 3 * width] = dv_ref[...].astype(BF16)
        out_ref[:, 3 * width: 3 * width + ch] = da_ref[...]
        out_ref[:, 3 * width + ch: out_w] = db_ref[...]

    vec = pl.BlockSpec((1, LANES), lambda i: (0, 0))
    blk = lambda c: pl.BlockSpec((tm, width), lambda i: (i, c))
    cblk = pl.BlockSpec((tm, ch), lambda i: (i, 0))
    return _pallas(
        body, name=name, grid=(s // tm,),
        in_specs=[ANY, blk(0), blk(0), blk(0), cblk, cblk, blk(0), blk(1),
                  pl.BlockSpec((tm, 1), lambda i: (i, 0)), vec, vec, vec, vec, vec,
                  pl.BlockSpec((LANES, LANES), lambda i: (0, 0))],
        out_specs=[pl.BlockSpec((tm, out_w), lambda i: (i, 0)), vec, vec],
        out_shape=[jax.ShapeDtypeStruct(dproj.shape, BF16)] + [jax.ShapeDtypeStruct((1, LANES), F32)] * 2,
        operands=[dproj, dqn, dkn, dv, da, db, proj, proj, pos_col, qw2, kw2, invf, m_a, m_b, bd],
        aliases={0: 0}, rider=rider)


def _row_chunks(n_rows, fn, chunk=256):
    def step(i, c):
        fn(pl.ds(pl.multiple_of(i * chunk, chunk), chunk))
        return c
    lax.fori_loop(0, n_rows // chunk, step, 0)


def _to_residue_major(dst, src, s, d, dst_off=0, cast=None):
    seq = s // d
    for r in range(d):
        v = src[...] if d == 1 else src[pl.ds(r, seq, stride=d), :]
        dst[dst_off + r * seq: dst_off + (r + 1) * seq, :] = v if cast is None else v.astype(cast)


def _from_residue_major(dst, src, s, d, src_off=0):
    seq = s // d
    for r in range(d):
        v = src[src_off + r * seq: src_off + (r + 1) * seq, :]
        if d == 1:
            dst[...] = v
        else:
            dst[pl.ds(r, seq, stride=d), :] = v


def _band_mask(base, seq):
    qi = lax.broadcasted_iota(jnp.int32, (QBLK, KWIN), 0)
    kj = lax.broadcasted_iota(jnp.int32, (QBLK, KWIN), 1)
    rel = kj - HALF_SPAN - qi
    s0 = jnp.bitwise_and(base, -seq)
    lo = s0 - base + HALF_SPAN
    return (jnp.abs(rel) <= HALF_SPAN) & (kj >= lo) & (kj < lo + seq)


def _attn_fwd(qn, kn, proj, name, rider=None):
    s = qn.shape[0]
    n_pairs = N_SLOT_HEADS * HEAD_DIM // LANES
    v_col0 = 2 * qn.shape[1] // LANES
    nt_dims = (((1,), (1,)), ((), ()))

    def body(q_ref, k_ref, v_ref, attn_ref, lse_ref, q_rm, k_rm, v_rm, acc_rm, m_rm, l_rm,
             acc_p, m_p, l_p, m_run, l_run, acc_run):
        g = pl.program_id(1)
        zpad = jnp.zeros((HALF_SPAN, LANES), BF16)
        for buf in (k_rm, v_rm):
            buf[0:HALF_SPAN, :] = zpad
            buf[s + HALF_SPAN: s + 2 * HALF_SPAN, :] = zpad
        lane = lax.broadcasted_iota(jnp.int32, (QBLK, LANES), 1)
        low = lane < HEAD_DIM

        for gi, d in enumerate(DILATIONS):
            @pl.when(g == gi)
            def _(gi=gi, d=d):
                seq = s // d
                _to_residue_major(q_rm, q_ref, s, d, cast=BF16)
                _to_residue_major(k_rm, k_ref, s, d, dst_off=HALF_SPAN, cast=BF16)
                _to_residue_major(v_rm, v_ref, s, d, dst_off=HALF_SPAN, cast=BF16)

                def qblock(b, carry):
                    base = pl.multiple_of(b * QBLK, QBLK)
                    q = q_rm[pl.ds(base, QBLK), :]
                    kw = k_rm[pl.ds(base, KWIN), :]
                    vw = v_rm[pl.ds(base, KWIN), :]
                    mask = _band_mask(base, seq)
                    outs = []
                    for hh in range(2):
                        qm = jnp.where(low if hh == 0 else jnp.logical_not(low), q, jnp.zeros_like(q))
                        sc = lax.dot_general(qm, kw, nt_dims, preferred_element_type=F32)
                        sc = jnp.where(mask, sc, NEG_INF)
                        m = jnp.max(sc, axis=-1, keepdims=True)
                        p = jnp.exp(sc - m)
                        l = jnp.sum(p, axis=-1, keepdims=True)
                        pv = jnp.dot(p.astype(BF16), vw, preferred_element_type=F32)
                        outs.append((m, l, pv))
                    rows = pl.ds(base, QBLK)
                    acc_rm[rows, :] = jnp.where(low, outs[0][2], outs[1][2])
                    m_rm[rows, :] = jnp.where(low, outs[0][0], outs[1][0])
                    l_rm[rows, :] = jnp.where(low, outs[0][1], outs[1][1])
                    return carry

                lax.fori_loop(0, s // QBLK, qblock, 0)
                if d == 1:
                    src = (acc_rm, m_rm, l_rm)
                else:
                    for dst_, src_ in ((acc_p, acc_rm), (m_p, m_rm), (l_p, l_rm)):
                        _from_residue_major(dst_, src_, s, d)
                    src = (acc_p, m_p, l_p)

                def combine(rows):
                    a_g, m_g, l_g = src[0][rows, :], src[1][rows, :], src[2][rows, :]
                    if gi == 0:
                        m_new, l_new, a_new = m_g, l_g, a_g
                    else:
                        m_old = m_run[rows, :]
                        m_new = jnp.maximum(m_old, m_g)
                        w_old = jnp.exp(m_old - m_new)
                        w_g = jnp.exp(m_g - m_new)
                        l_new = l_run[rows, :] * w_old + l_g * w_g
                        a_new = acc_run[rows, :] * w_old + a_g * w_g
                    if gi == len(DILATIONS) - 1:
                        attn_ref[rows, :] = a_new / l_new
                        lse_ref[rows, :] = m_new + jnp.log(l_new)
                    else:
                        m_run[rows, :] = m_new
                        l_run[rows, :] = l_new
                        acc_run[rows, :] = a_new

                _row_chunks(s, combine)

    qk_spec = pl.BlockSpec((s, LANES), lambda hp, g: (0, g * n_pairs + hp))
    v_spec = pl.BlockSpec((s, LANES), lambda hp, g: (0, v_col0 + g * n_pairs + hp))
    o_spec = pl.BlockSpec((s, LANES), lambda hp, g: (0, hp))
    f32buf = pltpu.VMEM((s, LANES), F32)
    return _pallas(
        body, name=name, grid=(n_pairs, len(DILATIONS)), in_specs=[qk_spec, qk_spec, v_spec],
        out_specs=[o_spec, o_spec],
        out_shape=[jax.ShapeDtypeStruct((s, n_pairs * LANES), F32)] * 2,
        operands=[qn, kn, proj],
        scratch_shapes=[pltpu.VMEM((s, LANES), BF16), pltpu.VMEM((s + 2 * HALF_SPAN, LANES), BF16),
                        pltpu.VMEM((s + 2 * HALF_SPAN, LANES), BF16)] + [f32buf] * 9,
        rider=rider)


def _attn_bwd(qn, kn, proj, dattn, attn, lse, bd, name, rider=None):
    s = qn.shape[0]
    n_pairs = N_SLOT_HEADS * HEAD_DIM // LANES
    v_col0 = 2 * qn.shape[1] // LANES
    nt_dims = (((1,), (1,)), ((), ()))
    tn_dims = (((0,), (0,)), ((), ()))
    spad = s + 2 * HALF_SPAN

    def body(q_ref, k_ref, v_ref, do_ref, o_ref, lse_ref, bd_ref, dq_ref, dk_ref, dv_ref,
             q_rm, k_rm, v_rm, do_rm, lse_rm, dd_rm, dq_rm, dk_rm, dv_rm, dd_p):
        g = pl.program_id(1)
        zpad = jnp.zeros((HALF_SPAN, LANES), BF16)
        for buf in (k_rm, v_rm):
            buf[0:HALF_SPAN, :] = zpad
            buf[s + HALF_SPAN: spad, :] = zpad
        zf = jnp.zeros((HALF_SPAN, LANES), F32)
        for buf in (dk_rm, dv_rm):
            buf[0:HALF_SPAN, :] = zf
            buf[s + HALF_SPAN: spad, :] = zf

        def zero_rows(rows):
            z = jnp.zeros((rows.size, LANES), F32)
            dk_rm[pl.ds(rows.start + HALF_SPAN, rows.size), :] = z
            dv_rm[pl.ds(rows.start + HALF_SPAN, rows.size), :] = z
            dd_p[rows, :] = _head_sums(do_ref[rows, :] * o_ref[rows, :], bd_ref[...])

        _row_chunks(s, zero_rows)
        lane = lax.broadcasted_iota(jnp.int32, (QBLK, LANES), 1)
        low = lane < HEAD_DIM

        for gi, d in enumerate(DILATIONS):
            @pl.when(g == gi)
            def _(d=d):
                seq = s // d
                _to_residue_major(q_rm, q_ref, s, d, cast=BF16)
                _to_residue_major(k_rm, k_ref, s, d, dst_off=HALF_SPAN, cast=BF16)
                _to_residue_major(v_rm, v_ref, s, d, dst_off=HALF_SPAN, cast=BF16)
                _to_residue_major(do_rm, do_ref, s, d, cast=BF16)
                _to_residue_major(lse_rm, lse_ref, s, d)
                _to_residue_major(dd_rm, dd_p, s, d)

                def qblock(b, carry):
                    base = pl.multiple_of(b * QBLK, QBLK)
                    rows = pl.ds(base, QBLK)
                    win = pl.ds(base, KWIN)
                    q = q_rm[rows, :]
                    do = do_rm[rows, :]
                    kw = k_rm[win, :]
                    vw = v_rm[win, :]
                    lse_b = lse_rm[rows, :]
                    dd_b = dd_rm[rows, :]
                    mask = _band_mask(base, seq)
                    dq_parts = []
                    dk_acc = jnp.zeros((KWIN, LANES), F32)
                    dv_acc = jnp.zeros((KWIN, LANES), F32)
                    for hh in range(2):
                        sel = low if hh == 0 else jnp.logical_not(low)
                        qm = jnp.where(sel, q, jnp.zeros_like(q))
                        dom = jnp.where(sel, do, jnp.zeros_like(do))
                        c0 = hh * HEAD_DIM
                        sc = lax.dot_general(qm, kw, nt_dims, preferred_element_type=F32)
                        sc = jnp.where(mask, sc, NEG_INF)
                        p = jnp.exp(sc - lse_b[:, c0:c0 + 1])
                        dp = lax.dot_general(dom, vw, nt_dims, preferred_element_type=F32)
                        ds = (p * (dp - dd_b[:, c0:c0 + 1])).astype(BF16)
                        dq_parts.append(jnp.dot(ds, kw, preferred_element_type=F32))
                        dk_acc = dk_acc + lax.dot_general(ds, qm, tn_dims, preferred_element_type=F32)
                        dv_acc = dv_acc + lax.dot_general(p.astype(BF16), dom, tn_dims,
                                                          preferred_element_type=F32)
                    dq_rm[rows, :] = jnp.where(low, dq_parts[0], dq_parts[1])
                    dk_rm[win, :] += dk_acc
                    dv_rm[win, :] += dv_acc
                    return carry

                lax.fori_loop(0, s // QBLK, qblock, 0)
                _from_residue_major(dq_ref, dq_rm, s, d)
                _from_residue_major(dk_ref, dk_rm, s, d, src_off=HALF_SPAN)
                _from_residue_major(dv_ref, dv_rm, s, d, src_off=HALF_SPAN)

    qk_spec = pl.BlockSpec((s, LANES), lambda hp, g: (0, g * n_pairs + hp))
    v_spec = pl.BlockSpec((s, LANES), lambda hp, g: (0, v_col0 + g * n_pairs + hp))
    o_spec = pl.BlockSpec((s, LANES), lambda hp, g: (0, hp))
    width = qn.shape[1]
    f32buf = pltpu.VMEM((s, LANES), F32)
    f32pad = pltpu.VMEM((spad, LANES), F32)
    return _pallas(
        body, name=name, grid=(n_pairs, len(DILATIONS)),
        in_specs=[qk_spec, qk_spec, v_spec, o_spec, o_spec, o_spec,
                  pl.BlockSpec((LANES, LANES), lambda hp, g: (0, 0))],
        out_specs=[qk_spec, qk_spec, qk_spec],
        out_shape=[jax.ShapeDtypeStruct((s, width), F32)] * 3,
        operands=[qn, kn, proj, dattn, attn, lse, bd],
        scratch_shapes=[pltpu.VMEM((s, LANES), BF16), pltpu.VMEM((spad, LANES), BF16),
                        pltpu.VMEM((spad, LANES), BF16), pltpu.VMEM((s, LANES), BF16),
                        f32buf, f32buf, f32buf, f32pad, f32pad, f32buf],
        rider=rider)


CONV_PAD = 16


def _conv_fwd(proj, conv_w, conv_b, col0, name, rider=None):
    s = proj.shape[0]
    ch = conv_w.shape[1]
    nblk = ch // LANES
    a0 = col0 // LANES
    tr = 256
    shift = CONV_PAD - (CONV_WIDTH - 1) // 2

    def body(a_ref, b_ref, w_ref, bias_ref, u0_ref, uc_ref, pad):
        z = jnp.zeros((CONV_PAD, LANES), F32)
        pad[0:CONV_PAD, :] = z
        pad[s + CONV_PAD: s + 2 * CONV_PAD, :] = z

        def glu(rows):
            u0 = a_ref[rows, :] * jax.nn.sigmoid(b_ref[rows, :])
            u0_ref[rows, :] = u0
            pad[pl.ds(rows.start + CONV_PAD, rows.size), :] = u0

        _row_chunks(s, glu)
        for t in range(0, s, tr):
            acc = jnp.broadcast_to(bias_ref[...], (tr, LANES))
            for k in range(CONV_WIDTH):
                acc = acc + w_ref[k:k + 1, :] * pad[t + k + shift: t + k + shift + tr, :]
            uc_ref[t:t + tr, :] = acc

    return _pallas(
        body, name=name, grid=(nblk,),
        in_specs=[pl.BlockSpec((s, LANES), lambda c: (0, a0 + c)),
                  pl.BlockSpec((s, LANES), lambda c: (0, a0 + nblk + c)),
                  pl.BlockSpec((CONV_WIDTH, LANES), lambda c: (0, c)),
                  pl.BlockSpec((1, LANES), lambda c: (0, c))],
        out_specs=[pl.BlockSpec((s, LANES), lambda c: (0, c))] * 2,
        out_shape=[jax.ShapeDtypeStruct((s, ch), F32)] * 2, operands=[proj, proj, conv_w, conv_b],
        scratch_shapes=[pltpu.VMEM((s + 2 * CONV_PAD, LANES), F32)], rider=rider)


def _ln_silu_fwd(uc, ln_w, ln_b, name):
    s, ch = uc.shape
    tm = 256

    def body(u_ref, w_ref, b_ref, o_ref):
        u = u_ref[...]
        mu = jnp.mean(u, axis=-1, keepdims=True)
        xc = u - mu
        rstd = lax.rsqrt(jnp.mean(xc * xc, axis=-1, keepdims=True) + EPS)
        z = xc * rstd * w_ref[...] + b_ref[...]
        o_ref[...] = (z * jax.nn.sigmoid(z)).astype(BF16)

    row = pl.BlockSpec((tm, ch), lambda i: (i, 0))
    vec = pl.BlockSpec((1, ch), lambda i: (0, 0))
    return pl.pallas_call(
        body, name=name, grid=(s // tm,), in_specs=[row, vec, vec], out_specs=row,
        out_shape=jax.ShapeDtypeStruct((s, ch), BF16), compiler_params=_params(),
    )(uc, ln_w, ln_b)


def _ln_silu_bwd(du3, uc, ln_w, ln_b, name):
    s, ch = uc.shape
    tm = 256

    def body(d_ref, u_ref, w_ref, b_ref, du_ref, dw_ref, db_ref):
        u = u_ref[...]
        mu = jnp.mean(u, axis=-1, keepdims=True)
        xc = u - mu
        rstd = lax.rsqrt(jnp.mean(xc * xc, axis=-1, keepdims=True) + EPS)
        xhat = xc * rstd
        z = xhat * w_ref[...] + b_ref[...]
        sg = jax.nn.sigmoid(z)
        dz = d_ref[...] * (sg * (1.0 + z * (1.0 - sg)))
        dxh = dz * w_ref[...]
        du_ref[...] = rstd * (dxh - jnp.mean(dxh, axis=-1, keepdims=True)
                              - xhat * jnp.mean(dxh * xhat, axis=-1, keepdims=True))
        pw = jnp.sum(dz * xhat, axis=0, keepdims=True)
        pb = jnp.sum(dz, axis=0, keepdims=True)
        first = pl.program_id(0) == 0

        @pl.when(first)
        def _():
            dw_ref[...] = pw
            db_ref[...] = pb

        @pl.when(jnp.logical_not(first))
        def _():
            dw_ref[...] += pw
            db_ref[...] += pb

    row = pl.BlockSpec((tm, ch), lambda i: (i, 0))
    vec = pl.BlockSpec((1, ch), lambda i: (0, 0))
    return pl.pallas_call(
        body, name=name, grid=(s // tm,), in_specs=[row, row, vec, vec], out_specs=[row, vec, vec],
        out_shape=[jax.ShapeDtypeStruct((s, ch), F32), jax.ShapeDtypeStruct((1, ch), F32),
                   jax.ShapeDtypeStruct((1, ch), F32)],
        compiler_params=_params(),
    )(du3, uc, ln_w, ln_b)


def _conv_bwd(duc, u0, proj, conv_w, col0, name, rider=None):
    s = proj.shape[0]
    ch = conv_w.shape[1]
    nblk = ch // LANES
    a0 = col0 // LANES
    tr = 256
    half = (CONV_WIDTH - 1) // 2
    shift = CONV_PAD - half

    def body(duc_ref, u0_ref, a_ref, b_ref, w_ref, da_ref, db_ref, dw_ref, dbias_ref, pad_d, pad_u):
        z = jnp.zeros((CONV_PAD, LANES), F32)
        for buf in (pad_d, pad_u):
            buf[0:CONV_PAD, :] = z
            buf[s + CONV_PAD: s + 2 * CONV_PAD, :] = z

        def fill(rows):
            dst = pl.ds(rows.start + CONV_PAD, rows.size)
            pad_d[dst, :] = duc_ref[rows, :]
            pad_u[dst, :] = u0_ref[rows, :]

        _row_chunks(s, fill)
        dw_acc = [jnp.zeros((8, LANES), F32) for _ in range(CONV_WIDTH)]
        dbias_acc = jnp.zeros((8, LANES), F32)
        for t in range(0, s, tr):
            d_t = duc_ref[t:t + tr, :]
            dbias_acc = dbias_acc + jnp.sum(d_t.reshape(tr // 8, 8, LANES), axis=0)
            du0 = jnp.zeros((tr, LANES), F32)
            for k in range(CONV_WIDTH):
                du0 = du0 + w_ref[k:k + 1, :] * pad_d[t - k + half + CONV_PAD: t - k + half + CONV_PAD + tr, :]
                prod = d_t * pad_u[t + k + shift: t + k + shift + tr, :]
                dw_acc[k] = dw_acc[k] + jnp.sum(prod.reshape(tr // 8, 8, LANES), axis=0)
            av = a_ref[t:t + tr, :]
            sg = jax.nn.sigmoid(b_ref[t:t + tr, :])
            da_ref[t:t + tr, :] = (du0 * sg).astype(BF16)
            db_ref[t:t + tr, :] = (du0 * av * sg * (1.0 - sg)).astype(BF16)
        for k in range(CONV_WIDTH):
            dw_ref[k:k + 1, :] = jnp.sum(dw_acc[k], axis=0, keepdims=True)
        dbias_ref[...] = jnp.sum(dbias_acc, axis=0, keepdims=True)

    col = lambda off: pl.BlockSpec((s, LANES), lambda c: (0, off + c))
    return _pallas(
        body, name=name, grid=(nblk,),
        in_specs=[col(0), col(0), col(a0), col(a0 + nblk),
                  pl.BlockSpec((CONV_WIDTH, LANES), lambda c: (0, c))],
        out_specs=[col(0), col(0), pl.BlockSpec((CONV_WIDTH, LANES), lambda c: (0, c)),
                   pl.BlockSpec((1, LANES), lambda c: (0, c))],
        out_shape=[jax.ShapeDtypeStruct((s, ch), BF16)] * 2
        + [jax.ShapeDtypeStruct((CONV_WIDTH, ch), F32), jax.ShapeDtypeStruct((1, ch), F32)],
        operands=[duc, u0, proj, proj, conv_w],
        scratch_shapes=[pltpu.VMEM((s + 2 * CONV_PAD, LANES), F32)] * 2, rider=rider)


GATE_BLK = 512


def _gate_fwd(proj, bg, y_a, y_b, col0, name):
    s, d = y_a.shape
    tm = 256
    g0 = col0 // GATE_BLK
    nb = d // GATE_BLK

    def body(ga_ref, gb_ref, ba_ref, bb_ref, ya_ref, yb_ref, o_ref):
        g_a = jax.nn.sigmoid(ga_ref[...] + ba_ref[...])
        g_b = jax.nn.sigmoid(gb_ref[...] + bb_ref[...])
        o_ref[...] = (g_a * ya_ref[...] + g_b * yb_ref[...]).astype(BF16)

    act = pl.BlockSpec((tm, GATE_BLK), lambda i, j: (i, j))
    return pl.pallas_call(
        body, name=name, grid=(s // tm, nb),
        in_specs=[pl.BlockSpec((tm, GATE_BLK), lambda i, j: (i, g0 + j)),
                  pl.BlockSpec((tm, GATE_BLK), lambda i, j: (i, g0 + nb + j)),
                  pl.BlockSpec((None, 1, GATE_BLK), lambda i, j: (0, 0, j)),
                  pl.BlockSpec((None, 1, GATE_BLK), lambda i, j: (1, 0, j)), act, act],
        out_specs=act, out_shape=jax.ShapeDtypeStruct((s, d), BF16), compiler_params=_params(),
    )(proj, proj, bg, bg, y_a, y_b)


def _gate_bwd(d_mixed, proj, bg, y_a, y_b, col0, name):
    s, d = y_a.shape
    tm = 256
    g0 = col0 // GATE_BLK
    nb = d // GATE_BLK

    def body(dm_ref, gl_ref, b_ref, y_ref, dgl_ref, dy_ref, db_ref):
        gate = jax.nn.sigmoid(gl_ref[...] + b_ref[...])
        dm = dm_ref[...]
        dy_ref[...] = (dm * gate).astype(BF16)
        dgl = dm * y_ref[...] * gate * (1.0 - gate)
        dgl_ref[...] = dgl.astype(BF16)
        part = jnp.sum(dgl, axis=0, keepdims=True)
        first = pl.program_id(1) == 0

        @pl.when(first)
        def _():
            db_ref[...] = part

        @pl.when(jnp.logical_not(first))
        def _():
            db_ref[...] += part

    def call(y, branch, dproj_prev):
        def wrapped(*refs):
            if dproj_prev is not None:
                refs = refs[1:]
            body(*refs)

        ins = [pl.BlockSpec((tm, GATE_BLK), lambda j, i: (i, j)),
               pl.BlockSpec((tm, GATE_BLK), lambda j, i: (i, g0 + branch * nb + j)),
               pl.BlockSpec((None, 1, GATE_BLK), lambda j, i: (branch, 0, j)),
               pl.BlockSpec((tm, GATE_BLK), lambda j, i: (i, j))]
        ops = [d_mixed, proj, bg, y]
        alias = {}
        if dproj_prev is not None:
            ins = [ANY] + ins
            ops = [dproj_prev] + ops
            alias = {0: 0}
        return pl.pallas_call(
            wrapped, name=f"{name}_{branch}", grid=(nb, s // tm), in_specs=ins,
            out_specs=[pl.BlockSpec((tm, GATE_BLK), lambda j, i: (i, g0 + branch * nb + j)),
                       pl.BlockSpec((tm, GATE_BLK), lambda j, i: (i, j)),
                       pl.BlockSpec((1, GATE_BLK), lambda j, i: (0, j))],
            out_shape=[jax.ShapeDtypeStruct((s, proj.shape[1]), BF16), jax.ShapeDtypeStruct((s, d), BF16),
                       jax.ShapeDtypeStruct((1, d), F32)],
            input_output_aliases=alias, compiler_params=_params(),
        )(*ops)

    dproj, dy_a, db_a = call(y_a, 0, None)
    dproj, dy_b, db_b = call(y_b, 1, dproj)
    return dproj, dy_a, dy_b, db_a, db_b


def _swiglu_fwd(gu, name):
    s, w2 = gu.shape
    ff = w2 // 2
    tm = 256
    tn = ff // 2

    def body(g_ref, u_ref, o_ref):
        gt = g_ref[...]
        o_ref[...] = (gt * jax.nn.sigmoid(gt) * u_ref[...]).astype(BF16)

    return pl.pallas_call(
        body, name=name, grid=(s // tm, 2),
        in_specs=[pl.BlockSpec((tm, tn), lambda i, j: (i, j)), pl.BlockSpec((tm, tn), lambda i, j: (i, 2 + j))],
        out_specs=pl.BlockSpec((tm, tn), lambda i, j: (i, j)),
        out_shape=jax.ShapeDtypeStruct((s, ff), BF16), compiler_params=_params(),
    )(gu, gu)


def _swiglu_bwd(gu, d_act, name):
    s, w2 = gu.shape
    ff = w2 // 2
    tm = 256
    tn = ff // 2

    def body(g_ref, u_ref, d_ref, o_ref):
        gt = g_ref[...]
        sg = jax.nn.sigmoid(gt)
        dv = d_ref[...]
        is_gate = pl.program_id(1) < 2

        @pl.when(is_gate)
        def _():
            o_ref[...] = (dv * u_ref[...] * (sg * (1.0 + gt * (1.0 - sg)))).astype(BF16)

        @pl.when(jnp.logical_not(is_gate))
        def _():
            o_ref[...] = (dv * gt * sg).astype(BF16)

    return pl.pallas_call(
        body, name=name, grid=(s // tm, 4),
        in_specs=[pl.BlockSpec((tm, tn), lambda i, j: (i, j % 2)),
                  pl.BlockSpec((tm, tn), lambda i, j: (i, 2 + j % 2)),
                  pl.BlockSpec((tm, tn), lambda i, j: (i, j % 2))],
        out_specs=pl.BlockSpec((tm, tn), lambda i, j: (i, j)),
        out_shape=jax.ShapeDtypeStruct((s, w2), BF16), compiler_params=_params(),
    )(gu, gu, d_act)


def _loss_fwd_bwd(y, target, name):
    s, d = y.shape
    tm = 256

    def body(y_ref, t_ref, dy_ref, dyb_ref, loss_ref, acc):
        diff = y_ref[...] - t_ref[...]
        dy = diff * (1.0 / d)
        dy_ref[...] = dy
        dyb_ref[...] = dy.astype(BF16)
        part = jnp.sum((diff * diff).reshape(tm // 8, 8, d), axis=0)
        i = pl.program_id(0)

        @pl.when(i == 0)
        def _():
            acc[...] = part

        @pl.when(i > 0)
        def _():
            acc[...] += part

        @pl.when(i == pl.num_programs(0) - 1)
        def _():
            loss_ref[...] = (0.5 / d) * jnp.sum(jnp.sum(acc[...], axis=1, keepdims=True), axis=0, keepdims=True)

    row = pl.BlockSpec((tm, d), lambda i: (i, 0))
    return pl.pallas_call(
        body, name=name, grid=(s // tm,), in_specs=[row, row],
        out_specs=[row, row, pl.BlockSpec((1, 1), lambda i: (0, 0))],
        out_shape=[jax.ShapeDtypeStruct((s, d), F32), jax.ShapeDtypeStruct((s, d), BF16),
                   jax.ShapeDtypeStruct((1, 1), F32)],
        scratch_shapes=[pltpu.VMEM((8, d), F32)], compiler_params=_params(),
    )(y, target)


LATE_GATHER = ("w_o_attn", "w_pw_conv", "w_out", "w_ffn_in", "w_ffn_out")
EARLY_REDUCE = LATE_GATHER


def _blocks_by_half(g):
    if g.ndim == 2:
        g = g.reshape(N_CHIPS, g.shape[0] // N_CHIPS, g.shape[1])
    return g.reshape(N_CHIPS, 2, g.shape[1] // 2, g.shape[2])


def _forward_backward(x, pos_col, target, wts, late_bufs, pos_arr):
    wts = dict(wts)
    consts = _rope_consts()
    bd = consts[3]
    qw2 = jnp.tile(wts["q_norm_w"], (1, LANES // HEAD_DIM))
    kw2 = jnp.tile(wts["k_norm_w"], (1, LANES // HEAD_DIM))
    qkv_w = 3 * N_SLOT_HEADS * HEAD_DIM
    conv_col0 = 3 * qkv_w
    ch = wts["conv_w"].shape[1]
    gate_col0 = conv_col0 + 2 * ch

    h = _rmsnorm_fwd(x, wts["norm1_w"], "rms1_fwd")
    proj = _matmul(h, wts["w_in"], mode="nn", tm=1024, tn=640, tk=1024, out_dtype=F32, name="mm_proj",
                   b_blocked=True)
    qn, kn = _qk_fwd(proj, pos_col, qw2, kw2, consts, "qk_fwd")
    (attn, lse), late_bufs = _attn_fwd(qn, kn, proj, "attn_fwd", rider=_gather_ici_rider(late_bufs, []))
    (u0, uc), late_bufs = _conv_fwd(proj, wts["conv_w"], wts["conv_b"], conv_col0, "conv_fwd",
                                    rider=_gather_forward_rider(late_bufs))
    for n, buf in zip(LATE_GATHER, late_bufs):
        full = buf.reshape(N_CHIPS, -1, buf.shape[3])
        wts[n] = full.reshape(-1, full.shape[2]) if n in ROW_SHARDED else full
    attn_b = attn.astype(BF16)
    y_a = _matmul(attn_b, wts["w_o_attn"], mode="nn", tm=1024, tn=256, tk=512, out_dtype=F32, name="mm_ya",
                  b_blocked=True)
    u3 = _ln_silu_fwd(uc, wts["conv_ln_w"], wts["conv_ln_b"], "ln_fwd")
    y_b = _matmul(u3, wts["w_pw_conv"], mode="nn", tm=1024, tn=256, tk=512, out_dtype=F32, name="mm_yb",
                  b_blocked=True)
    mixed = _gate_fwd(proj, wts["b_gate"], y_a, y_b, gate_col0, "gate_fwd")
    x1 = _matmul(mixed, wts["w_out"], mode="nn", tm=1024, tn=512, tk=1024, out_dtype=F32, name="mm_x1",
                 residual=x)
    h2 = _rmsnorm_fwd(x1, wts["norm2_w"], "rms2_fwd")
    gu = _matmul(h2, wts["w_ffn_in"], mode="nn", tm=512, tn=1408, tk=1024, out_dtype=F32, name="mm_gu",
                 b_blocked=True)
    act = _swiglu_fwd(gu, "swiglu_fwd")
    x2 = _matmul(act, wts["w_ffn_out"], mode="nn", tm=512, tn=512, tk=2816, out_dtype=F32, name="mm_x2",
                 residual=x1)
    dy, dy_b16, loss = _loss_fwd_bwd(x2, target, "loss")

    g = {}
    d_act = _matmul(dy_b16, wts["w_ffn_out"], mode="nt", tm=512, tn=1408, tk=1024, out_dtype=F32, name="mm_dact")
    g["w_ffn_out"] = _matmul(act, dy_b16, mode="tn", tm=1408, tn=512, tk=2048, out_dtype=F32, name="mm_dwffnout")
    dgu = _swiglu_bwd(gu, d_act, "swiglu_bwd")
    dh2 = _matmul(dgu, wts["w_ffn_in"], mode="nt", tm=512, tn=1024, tk=1408, out_dtype=F32, name="mm_dh2",
                  b_blocked=True)
    g["w_ffn_in"] = _matmul(h2, dgu, mode="tn", tm=512, tn=1408, tk=2048, out_dtype=F32, name="mm_dwffnin",
                            out_blocked=N_CHIPS)
    dx1, dx1_b16, g["norm2_w"] = _rmsnorm_bwd(dh2, x1, wts["norm2_w"], dy, "rms2_bwd")
    d_mixed = _matmul(dx1_b16, wts["w_out"], mode="nt", tm=1024, tn=512, tk=1024, out_dtype=F32, name="mm_dmixed")
    g["w_out"] = _matmul(mixed, dx1_b16, mode="tn", tm=512, tn=1024, tk=2048, out_dtype=F32, name="mm_dwout")
    dproj, dy_a, dy_b, db_a, db_b = _gate_bwd(d_mixed, proj, wts["b_gate"], y_a, y_b, gate_col0, "gate_bwd")
    g["b_gate"] = jnp.concatenate([db_a, db_b], axis=0)
    dattn = _matmul(dy_a, wts["w_o_attn"], mode="nt", tm=1024, tn=512, tk=256, out_dtype=F32, name="mm_dattn",
                    b_blocked=True)
    g["w_o_attn"] = _matmul(attn_b, dy_a, mode="tn", tm=512, tn=256, tk=2048, out_dtype=F32, name="mm_dwo",
                            out_blocked=N_CHIPS)
    du3 = _matmul(dy_b, wts["w_pw_conv"], mode="nt", tm=1024, tn=512, tk=256, out_dtype=F32, name="mm_du3",
                  b_blocked=True)
    g["w_pw_conv"] = _matmul(u3, dy_b, mode="tn", tm=512, tn=256, tk=2048, out_dtype=F32, name="mm_dwpw",
                             out_blocked=N_CHIPS)
    duc, g["conv_ln_w"], g["conv_ln_b"] = _ln_silu_bwd(du3, uc, wts["conv_ln_w"], wts["conv_ln_b"], "ln_bwd")

    early = [_blocks_by_half(g.pop(n)) for n in EARLY_REDUCE]
    (da, db, g["conv_w"], g["conv_b"]), from_sibling = _conv_bwd(
        duc, u0, proj, wts["conv_w"], conv_col0, "conv_bwd", rider=_pair_exchange_rider(early, halved=True))
    sums = [_add_own_half(gb, rv, pos_arr, f"grads_pair_add_{n}")
            for n, gb, rv in zip(EARLY_REDUCE, early, from_sibling)]
    (dqn, dkn, dv), by_chip = _attn_bwd(
        qn, kn, proj, dattn, attn, lse, bd, "attn_bwd",
        rider=_chip_exchange_rider([s[0] for s in sums], [s[1] for s in sums]))
    halves = [_sum_chips(bc, pos_arr, f"grads_chip_sum_{n}") for n, bc in zip(EARLY_REDUCE, by_chip)]
    (dproj, dqw, dkw), shards = _qk_bwd(dproj, dqn, dkn, dv, da, db, proj, pos_col, qw2, kw2, consts, "qk_bwd",
                                        rider=_pair_gather_rider(halves))
    reduced = dict(zip(EARLY_REDUCE, shards))
    g["q_norm_w"] = dqw[:, :HEAD_DIM]
    g["k_norm_w"] = dkw[:, :HEAD_DIM]

    c = pos_arr[0]
    rh = h.shape[1] // 2
    h_sibling = lax.dynamic_slice_in_dim(h, (1 - c) * rh, rh, axis=1)
    h_own = lax.dynamic_slice_in_dim(h, c * rh, rh, axis=1)
    g_sibling = _matmul(h_sibling, dproj, mode="tn", tm=rh, tn=640, tk=2048, out_dtype=F32, name="mm_dwin_sibling",
                        out_blocked=N_CHIPS)
    g_own, from_sibling = _matmul(h_own, dproj, mode="tn", tm=rh, tn=640, tk=2048, out_dtype=F32,
                                  name="mm_dwin_own", out_blocked=N_CHIPS,
                                  rider=_pair_exchange_rider([g_sibling], halved=False))
    to_send, own = _add_own_half(g_own, from_sibling[0], pos_arr, "grads_pair_add_w_in")
    dh, by_chip_w_in = _matmul(dproj, wts["w_in"], mode="nt", tm=512, tn=1024, tk=1920, out_dtype=F32, name="mm_dh",
                               b_blocked=True, rider=_chip_exchange_rider([to_send], [own]))
    grad_x, _, g["norm1_w"] = _rmsnorm_bwd(dh, x, wts["norm1_w"], dx1, "rms1_bwd")
    return loss, grad_x, g, reduced, by_chip_w_in[0]


def _mesh_pos():
    return lax.axis_index("x"), lax.axis_index("y"), lax.axis_index("c")


def _other_chips(x, y):
    return [(1 - x, y), (x, 1 - y), (1 - x, 1 - y)]


def _cast_into_slot(shard, chip_arr, dtype, name):
    r, c = shard.shape
    tr = r // 2 if r % 32 == 0 else r

    def body(chip_ref, s_ref, o_ref):
        del chip_ref
        o_ref[...] = s_ref[...].astype(dtype)

    return pl.pallas_call(
        body, name=name,
        grid_spec=pltpu.PrefetchScalarGridSpec(
            num_scalar_prefetch=1, grid=(r // tr,),
            in_specs=[pl.BlockSpec((tr, c), lambda i, chip_ref: (i, 0))],
            out_specs=pl.BlockSpec((None, tr, c), lambda i, chip_ref: (chip_ref[0], i, 0))),
        out_shape=jax.ShapeDtypeStruct((N_CHIPS, r, c), dtype), compiler_params=_params(),
    )(chip_arr, shard)


def _allgather_inplace(big, small, name):
    nb, ns = len(big), len(small)
    n = nb + ns

    def body(*refs):
        bufs = refs[n:2 * n]
        send_sems, recv_sems, fsend_sems, frecv_sems = refs[2 * n:]
        x, y, c = _mesh_pos()
        me = 2 * x + y
        chips = _other_chips(x, y)

        def part(a, slot, half):
            return bufs[a].at[slot, half] if a < nb else bufs[a].at[slot]

        sends = []
        for a in range(n):
            for k, (px, py) in enumerate(chips):
                cp = pltpu.make_async_remote_copy(
                    src_ref=part(a, me, c), dst_ref=part(a, me, c), send_sem=send_sems.at[a, k],
                    recv_sem=recv_sems.at[a, k], device_id=(px, py, c), device_id_type=MESH)
                cp.start()
                sends.append(cp)
        for a in range(n):
            for k, (px, py) in enumerate(chips):
                slot = 2 * px + py
                pltpu.make_async_remote_copy(
                    src_ref=part(a, slot, c), dst_ref=part(a, slot, c), send_sem=send_sems.at[a, k],
                    recv_sem=recv_sems.at[a, k], device_id=(px, py, c), device_id_type=MESH).wait_recv()
                if a < nb:
                    fwd = pltpu.make_async_remote_copy(
                        src_ref=part(a, slot, c), dst_ref=part(a, slot, c), send_sem=fsend_sems.at[a, k],
                        recv_sem=frecv_sems.at[a, k], device_id=(x, y, 1 - c), device_id_type=MESH)
                    fwd.start()
                    sends.append(fwd)
        for a in range(nb):
            for k, (px, py) in enumerate(chips):
                slot = 2 * px + py
                pltpu.make_async_remote_copy(
                    src_ref=part(a, slot, 1 - c), dst_ref=part(a, slot, 1 - c), send_sem=fsend_sems.at[a, k],
                    recv_sem=frecv_sems.at[a, k], device_id=(x, y, 1 - c), device_id_type=MESH).wait_recv()
        for cp in sends:
            cp.wait_send()

    ops = list(big) + list(small)
    return pl.pallas_call(
        body, name=name, in_specs=[ANY] * n, out_specs=[ANY] * n,
        out_shape=[jax.ShapeDtypeStruct(o.shape, o.dtype) for o in ops],
        input_output_aliases={i: i for i in range(n)},
        scratch_shapes=[pltpu.SemaphoreType.DMA((n, 3)), pltpu.SemaphoreType.DMA((n, 3)),
                        pltpu.SemaphoreType.DMA((nb, 3)), pltpu.SemaphoreType.DMA((nb, 3))],
    )(*ops)


def _comm_call(rider, name):
    def body():
        pass

    return _pallas(body, name=name, grid=(1,), in_specs=[], out_specs=[], out_shape=[], operands=[],
                   rider=rider)[1]


def _gather_ici_rider(big, small):
    nb = len(big)
    n = nb + len(small)

    def copies(bufs, sems):
        x, y, c = _mesh_pos()
        me = 2 * x + y
        part = lambda a, slot: bufs[a].at[slot, c] if a < nb else bufs[a].at[slot]
        out = []
        for a in range(n):
            for k, (px, py) in enumerate(_other_chips(x, y)):
                send = pltpu.make_async_remote_copy(
                    src_ref=part(a, me), dst_ref=part(a, me), send_sem=sems[0].at[a, k],
                    recv_sem=sems[1].at[a, k], device_id=(px, py, c), device_id_type=MESH)
                recv = pltpu.make_async_remote_copy(
                    src_ref=part(a, 2 * px + py), dst_ref=part(a, 2 * px + py), send_sem=sems[0].at[a, k],
                    recv_sem=sems[1].at[a, k], device_id=(px, py, c), device_id_type=MESH)
                out.append((send, recv))
        return out

    def start(r_in, r_out, sems):
        for send, _ in copies(r_out, sems):
            send.start()

    def wait(r_in, r_out, sems):
        cps = copies(r_out, sems)
        for _, recv in cps:
            recv.wait_recv()
        for send, _ in cps:
            send.wait_send()

    ops = list(big) + list(small)
    return _Rider(ops, [jax.ShapeDtypeStruct(o.shape, o.dtype) for o in ops], {i: i for i in range(n)},
                  [pltpu.SemaphoreType.DMA((n, 3)), pltpu.SemaphoreType.DMA((n, 3))], start, wait)


def _gather_forward_rider(big):
    n = len(big)

    def copies(bufs, sems):
        x, y, c = _mesh_pos()
        out = []
        for a in range(n):
            for k, (px, py) in enumerate(_other_chips(x, y)):
                slot = 2 * px + py
                send = pltpu.make_async_remote_copy(
                    src_ref=bufs[a].at[slot, c], dst_ref=bufs[a].at[slot, c], send_sem=sems[0].at[a, k],
                    recv_sem=sems[1].at[a, k], device_id=(x, y, 1 - c), device_id_type=MESH)
                recv = pltpu.make_async_remote_copy(
                    src_ref=bufs[a].at[slot, 1 - c], dst_ref=bufs[a].at[slot, 1 - c], send_sem=sems[0].at[a, k],
                    recv_sem=sems[1].at[a, k], device_id=(x, y, 1 - c), device_id_type=MESH)
                out.append((send, recv))
        return out

    def start(r_in, r_out, sems):
        for send, _ in copies(r_out, sems):
            send.start()

    def wait(r_in, r_out, sems):
        cps = copies(r_out, sems)
        for _, recv in cps:
            recv.wait_recv()
        for send, _ in cps:
            send.wait_send()

    return _Rider(big, [jax.ShapeDtypeStruct(o.shape, o.dtype) for o in big], {i: i for i in range(n)},
                  [pltpu.SemaphoreType.DMA((n, 3)), pltpu.SemaphoreType.DMA((n, 3))], start, wait)


def _pair_exchange_rider(gs, halved):
    n = len(gs)

    def copies(r_in, r_out, sems):
        x, y, c = _mesh_pos()
        return [pltpu.make_async_remote_copy(
            src_ref=r_in[a].at[:, 1 - c] if halved else r_in[a], dst_ref=r_out[a], send_sem=sems[0].at[a],
            recv_sem=sems[1].at[a], device_id=(x, y, 1 - c), device_id_type=MESH) for a in range(n)]

    def start(r_in, r_out, sems):
        for cp in copies(r_in, r_out, sems):
            cp.start()

    def wait(r_in, r_out, sems):
        for cp in copies(r_in, r_out, sems):
            cp.wait()

    return _Rider(gs, [jax.ShapeDtypeStruct((g.shape[0],) + g.shape[-2:], g.dtype) for g in gs], {},
                  [pltpu.SemaphoreType.DMA((n,)), pltpu.SemaphoreType.DMA((n,))], start, wait)


def _chip_exchange_rider(to_send, by_chip):
    n = len(to_send)

    def copies(r_in, r_out, sems):
        x, y, c = _mesh_pos()
        me = 2 * x + y
        out = []
        for a in range(n):
            for k, (px, py) in enumerate(_other_chips(x, y)):
                send = pltpu.make_async_remote_copy(
                    src_ref=r_in[a].at[2 * px + py], dst_ref=r_out[a].at[me], send_sem=sems[0].at[a, k],
                    recv_sem=sems[1].at[a, k], device_id=(px, py, c), device_id_type=MESH)
                recv = pltpu.make_async_remote_copy(
                    src_ref=r_in[a].at[me], dst_ref=r_out[a].at[2 * px + py], send_sem=sems[0].at[a, k],
                    recv_sem=sems[1].at[a, k], device_id=(px, py, c), device_id_type=MESH)
                out.append((send, recv))
        return out

    def start(r_in, r_out, sems):
        for send, _ in copies(r_in, r_out, sems):
            send.start()

    def wait(r_in, r_out, sems):
        cps = copies(r_in, r_out, sems)
        for _, recv in cps:
            recv.wait_recv()
        for send, _ in cps:
            send.wait_send()

    return _Rider(list(to_send) + list(by_chip), [jax.ShapeDtypeStruct(b.shape, b.dtype) for b in by_chip],
                  {n + i: i for i in range(n)},
                  [pltpu.SemaphoreType.DMA((n, 3)), pltpu.SemaphoreType.DMA((n, 3))], start, wait)


def _pair_gather_rider(bufs):
    n = len(bufs)

    def copies(r_out, sems):
        x, y, c = _mesh_pos()
        out = []
        for a in range(n):
            send = pltpu.make_async_remote_copy(
                src_ref=r_out[a].at[c], dst_ref=r_out[a].at[c], send_sem=sems[0].at[a],
                recv_sem=sems[1].at[a], device_id=(x, y, 1 - c), device_id_type=MESH)
            recv = pltpu.make_async_remote_copy(
                src_ref=r_out[a].at[1 - c], dst_ref=r_out[a].at[1 - c], send_sem=sems[0].at[a],
                recv_sem=sems[1].at[a], device_id=(x, y, 1 - c), device_id_type=MESH)
            out.append((send, recv))
        return out

    def start(r_in, r_out, sems):
        for send, _ in copies(r_out, sems):
            send.start()

    def wait(r_in, r_out, sems):
        cps = copies(r_out, sems)
        for _, recv in cps:
            recv.wait_recv()
        for send, _ in cps:
            send.wait_send()

    return _Rider(bufs, [jax.ShapeDtypeStruct(b.shape, b.dtype) for b in bufs], {i: i for i in range(n)},
                  [pltpu.SemaphoreType.DMA((n,)), pltpu.SemaphoreType.DMA((n,))], start, wait)


def _add_own_half(g, recv, pos_arr, name):
    nb, rh, cols = g.shape[0], g.shape[-2], g.shape[-1]

    def body(pos_ref, g_ref, r_ref, send_ref, own_ref):
        s = (g_ref[...] + r_ref[...]).astype(BF16)
        send_ref[...] = s

        @pl.when(pl.program_id(0) == pos_ref[1])
        def _():
            own_ref[...] = s

    blk = pl.BlockSpec((None, rh, cols), lambda j, pos_ref: (j, 0, 0))
    g_spec = blk if g.ndim == 3 else pl.BlockSpec((None, None, rh, cols),
                                                   lambda j, pos_ref: (j, pos_ref[0], 0, 0))
    shape = jax.ShapeDtypeStruct((nb, rh, cols), BF16)
    return pl.pallas_call(
        body, name=name,
        grid_spec=pltpu.PrefetchScalarGridSpec(
            num_scalar_prefetch=1, grid=(nb,), in_specs=[g_spec, blk],
            out_specs=[blk, pl.BlockSpec((None, rh, cols), lambda j, pos_ref: (pos_ref[1], 0, 0))]),
        out_shape=[shape, shape], compiler_params=_params(),
    )(pos_arr, g, recv)


def _sum_chips(gath, pos_arr, name):
    nb, rh, cols = gath.shape

    def body(pos_ref, a_ref, b_ref, c_ref, d_ref, o_ref):
        del pos_ref
        o_ref[...] = ((a_ref[...].astype(F32) + b_ref[...].astype(F32)) + c_ref[...].astype(F32)) \
            + d_ref[...].astype(F32)

    tr = rh // 2 if (rh // 2) % 16 == 0 else rh
    specs = [pl.BlockSpec((None, tr, cols), functools.partial(lambda i, pos_ref, j: (j, i, 0), j=j))
             for j in range(nb)]
    return pl.pallas_call(
        body, name=name,
        grid_spec=pltpu.PrefetchScalarGridSpec(
            num_scalar_prefetch=1, grid=(rh // tr,), in_specs=specs,
            out_specs=pl.BlockSpec((None, tr, cols), lambda i, pos_ref: (pos_ref[0], i, 0))),
        out_shape=jax.ShapeDtypeStruct((2, rh, cols), F32), compiler_params=_params(),
    )(pos_arr, gath, gath, gath, gath)


def _small_allreduce(v, name):
    n = v.shape[0]
    n_dev = 8

    def body(v_ref, o_ref, buf, send_sems, recv_sems):
        x, y, c = _mesh_pos()
        me = 4 * x + 2 * y + c
        buf[me] = v_ref[...]
        sends = []
        peers = []
        for r in range(1, n_dev):
            px = 1 - x if r & 4 else x
            py = 1 - y if r & 2 else y
            pc = 1 - c if r & 1 else c
            peers.append((px, py, pc))
            cp = pltpu.make_async_remote_copy(
                src_ref=v_ref, dst_ref=buf.at[me], send_sem=send_sems.at[r - 1],
                recv_sem=recv_sems.at[r - 1], device_id=(px, py, pc), device_id_type=MESH)
            cp.start()
            sends.append(cp)
        for r, (px, py, pc) in enumerate(peers):
            pltpu.make_async_remote_copy(
                src_ref=v_ref, dst_ref=buf.at[4 * px + 2 * py + pc], send_sem=send_sems.at[r],
                recv_sem=recv_sems.at[r], device_id=(px, py, pc), device_id_type=MESH).wait_recv()
        for cp in sends:
            cp.wait_send()
        acc = buf[0]
        for i in range(1, n_dev):
            acc = acc + buf[i]
        o_ref[...] = acc

    vm = pl.BlockSpec(memory_space=pltpu.VMEM)
    return pl.pallas_call(
        body, name=name, in_specs=[vm], out_specs=vm, out_shape=jax.ShapeDtypeStruct(v.shape, v.dtype),
        scratch_shapes=[pltpu.VMEM((n_dev, n, LANES), F32), pltpu.SemaphoreType.DMA((n_dev - 1,)),
                        pltpu.SemaphoreType.DMA((n_dev - 1,))],
        compiler_params=_params(),
    )(v)


def _adamw_math(w, g, m, v):
    m = ADAM_B1 * m + (1.0 - ADAM_B1) * g
    v = ADAM_B2 * v + (1.0 - ADAM_B2) * (g * g)
    m_hat = m / (1.0 - ADAM_B1 ** ADAM_STEP)
    v_hat = v / (1.0 - ADAM_B2 ** ADAM_STEP)
    delta = -ADAM_LR * (m_hat / (jnp.sqrt(v_hat) + ADAM_EPS) + ADAM_WD * w)
    return delta, m, v


def _adamw(w, g, m, v, name, rider=None):
    r, c = w.shape
    tr = 128 if r % 128 == 0 else 64
    assert r % tr == 0

    def body(w_ref, g_ref, m_ref, v_ref, d_ref, mo_ref, vo_ref):
        d, mn, vn = _adamw_math(w_ref[...], g_ref[...], m_ref[...], v_ref[...])
        d_ref[...] = d
        mo_ref[...] = mn
        vo_ref[...] = vn

    blk = pl.BlockSpec((tr, c), lambda i: (i, 0))
    return _pallas(body, name=name, grid=(r // tr,), in_specs=[blk] * 4, out_specs=[blk] * 3,
                   out_shape=[jax.ShapeDtypeStruct((r, c), F32)] * 3, operands=[w, g, m, v], rider=rider)


def _adamw_small(ws, gs, ms, vs, name):
    n = len(ws)

    def body(*refs):
        w_r, g_r, m_r, v_r = refs[:n], refs[n:2 * n], refs[2 * n:3 * n], refs[3 * n:4 * n]
        d_o, m_o, v_o = refs[4 * n:5 * n], refs[5 * n:6 * n], refs[6 * n:7 * n]
        for i in range(n):
            d, mn, vn = _adamw_math(w_r[i][...], g_r[i][...], m_r[i][...], v_r[i][...])
            d_o[i][...] = d
            m_o[i][...] = mn
            v_o[i][...] = vn

    vm = pl.BlockSpec(memory_space=pltpu.VMEM)
    shapes = [jax.ShapeDtypeStruct(w.shape, F32) for w in ws]
    outs = pl.pallas_call(
        body, name=name, in_specs=[vm] * (4 * n), out_specs=[vm] * (3 * n), out_shape=shapes * 3,
        compiler_params=_params(),
    )(*ws, *gs, *ms, *vs)
    return outs[:n], outs[n:2 * n], outs[2 * n:]


BIG = ("w_in", "w_o_attn", "w_pw_conv", "w_out", "w_ffn_in", "w_ffn_out")
ROW_SHARDED = ("w_out", "w_ffn_out")
SMALL = ("norm1_w", "b_gate", "q_norm_w", "k_norm_w", "conv_w", "conv_b", "conv_ln_w", "conv_ln_b", "norm2_w")
ORDER = ("norm1_w", "w_in", "b_gate", "q_norm_w", "k_norm_w", "w_o_attn", "conv_w", "conv_b", "conv_ln_w",
         "conv_ln_b", "w_pw_conv", "w_out", "norm2_w", "w_ffn_in", "w_ffn_out")
PACK_TILE = 8 * LANES


def _pack_small(parts):
    rows = []
    for p in parts:
        flat = p.reshape(-1)
        pad = (-flat.shape[0]) % PACK_TILE
        rows.append(jnp.pad(flat, (0, pad)).reshape(-1, LANES))
    return jnp.concatenate(rows, axis=0)


def _unpack_small(packed, shapes):
    out, row = [], 0
    for shp in shapes:
        size = int(np.prod(shp))
        nrow = -(-size // PACK_TILE) * (PACK_TILE // LANES)
        out.append(packed[row:row + nrow].reshape(-1)[:size].reshape(shp))
        row += nrow
    return out


def kernel(x, positions, norm1_w, w_in, b_gate, q_norm_w, k_norm_w, w_o_attn, conv_w, conv_b, conv_ln_w, conv_ln_b, w_pw_conv, w_out, norm2_w, w_ffn_in, w_ffn_out, loss_target, m_norm1_w, m_w_in, m_b_gate, m_q_norm_w, m_k_norm_w, m_w_o_attn, m_conv_w, m_conv_b, m_conv_ln_w, m_conv_ln_b, m_w_pw_conv, m_w_out, m_norm2_w, m_w_ffn_in, m_w_ffn_out, v_norm1_w, v_w_in, v_b_gate, v_q_norm_w, v_k_norm_w, v_w_o_attn, v_conv_w, v_conv_b, v_conv_ln_w, v_conv_ln_b, v_w_pw_conv, v_w_out, v_norm2_w, v_w_ffn_in, v_w_ffn_out):
    w = dict(norm1_w=norm1_w, w_in=w_in, b_gate=b_gate, q_norm_w=q_norm_w, k_norm_w=k_norm_w, w_o_attn=w_o_attn,
             conv_w=conv_w, conv_b=conv_b, conv_ln_w=conv_ln_w, conv_ln_b=conv_ln_b, w_pw_conv=w_pw_conv,
             w_out=w_out, norm2_w=norm2_w, w_ffn_in=w_ffn_in, w_ffn_out=w_ffn_out)
    m = dict(norm1_w=m_norm1_w, w_in=m_w_in, b_gate=m_b_gate, q_norm_w=m_q_norm_w, k_norm_w=m_k_norm_w,
             w_o_attn=m_w_o_attn, conv_w=m_conv_w, conv_b=m_conv_b, conv_ln_w=m_conv_ln_w,
             conv_ln_b=m_conv_ln_b, w_pw_conv=m_w_pw_conv, w_out=m_w_out, norm2_w=m_norm2_w,
             w_ffn_in=m_w_ffn_in, w_ffn_out=m_w_ffn_out)
    v = dict(norm1_w=v_norm1_w, w_in=v_w_in, b_gate=v_b_gate, q_norm_w=v_q_norm_w, k_norm_w=v_k_norm_w,
             w_o_attn=v_w_o_attn, conv_w=v_conv_w, conv_b=v_conv_b, conv_ln_w=v_conv_ln_w,
             conv_ln_b=v_conv_ln_b, w_pw_conv=v_w_pw_conv, w_out=v_w_out, norm2_w=v_norm2_w,
             w_ffn_in=v_w_ffn_in, w_ffn_out=v_w_ffn_out)
    cx, cy, cc = _mesh_pos()
    chip = 2 * cx + cy

    chip_arr = chip.reshape(1).astype(jnp.int32)
    pos_arr = jnp.stack([cc, chip]).astype(jnp.int32)
    bufs = {}
    for n in BIG:
        buf = _cast_into_slot(w[n][0], chip_arr, BF16, f"cast_{n}")
        bufs[n] = buf.reshape(N_CHIPS, 2, buf.shape[1] // 2, buf.shape[2])
    small_bufs = [_cast_into_slot(w[n][0], chip_arr, F32, f"slot_{n}") for n in ("conv_w", "b_gate")]
    w_in_buf, conv_w_buf, b_gate_buf = _allgather_inplace([bufs["w_in"]], small_bufs, "allgather_w_in")
    wts = dict(w_in=w_in_buf.reshape(N_CHIPS, -1, w_in_buf.shape[3]),
               conv_w=conv_w_buf.transpose(1, 0, 2).reshape(CONV_WIDTH, -1),
               b_gate=b_gate_buf.transpose(1, 0, 2).reshape(2, 1, -1),
               norm1_w=norm1_w, q_norm_w=q_norm_w, k_norm_w=k_norm_w, conv_b=conv_b, conv_ln_w=conv_ln_w,
               conv_ln_b=conv_ln_b, norm2_w=norm2_w)

    loss, grad_x, g, reduced, by_chip_w_in = _forward_backward(
        x[0], positions.reshape(-1, 1), loss_target[0], wts, [bufs[n] for n in LATE_GATHER], pos_arr)
    grads = {n: b.reshape(-1, b.shape[2]) for n, b in reduced.items()}
    half_w_in = _sum_chips(by_chip_w_in, pos_arr, "grads_chip_sum_w_in")

    small_parts = [loss] + [g[n] for n in SMALL]
    small_shapes = [p.shape for p in small_parts]
    reduced = _unpack_small(_small_allreduce(_pack_small(small_parts), "small_allreduce"), small_shapes)
    loss_total = reduced[0].reshape(())
    for n, r in zip(SMALL, reduced[1:]):
        grads[n] = r
    ch_shard = conv_w.shape[2]
    grads["conv_w"] = lax.dynamic_slice_in_dim(grads["conv_w"], chip * ch_shard, ch_shard, axis=1)
    d_shard = b_gate.shape[2]
    grads["b_gate"] = lax.dynamic_slice_in_dim(grads["b_gate"], chip * d_shard, d_shard, axis=1)

    delta, new_m, new_v = {}, {}, {}
    for n in EARLY_REDUCE:
        rider = _pair_gather_rider([half_w_in]) if n == "w_ffn_in" else None
        res = _adamw(w[n][0], grads[n], m[n][0], v[n][0], f"adamw_{n}", rider=rider)
        if rider is not None:
            res, (shard_w_in,) = res
            grads["w_in"] = shard_w_in.reshape(-1, shard_w_in.shape[2])
        delta[n], new_m[n], new_v[n] = res
    delta["w_in"], new_m["w_in"], new_v["w_in"] = _adamw(w["w_in"][0], grads["w_in"], m["w_in"][0], v["w_in"][0],
                                                          "adamw_w_in")
    flat2 = lambda a: a.reshape(-1, a.shape[-1])
    d_s, m_s, v_s = _adamw_small([flat2(w[n]) for n in SMALL], [flat2(grads[n]) for n in SMALL],
                                 [flat2(m[n]) for n in SMALL], [flat2(v[n]) for n in SMALL], "adamw_small")
    for i, n in enumerate(SMALL):
        delta[n], new_m[n], new_v[n] = d_s[i], m_s[i], v_s[i]

    shaped = lambda d, n: d[n].reshape(w[n].shape)
    return (loss_total, grad_x[None], *[shaped(grads, n) for n in ORDER], *[shaped(delta, n) for n in ORDER],
            *[shaped(new_m, n) for n in ORDER], *[shaped(new_v, n) for n in ORDER])
```

```python
import functools

import numpy as np
import jax
import jax.numpy as jnp
from jax import lax
from jax.experimental import pallas as pl
from jax.experimental.pallas import tpu as pltpu

F32 = jnp.float32
BF16 = jnp.bfloat16
MESH = pl.DeviceIdType.MESH
ANY = pl.BlockSpec(memory_space=pl.ANY)

HEAD_DIM = 64
N_SLOT_HEADS = 8
DILATIONS = (1, 4, 16)
HALF_SPAN = 64
ROPE_THETA = 500000.0
ROT_DIM = 16
CONV_WIDTH = 31
EPS = 1e-6
NEG_INF = -1e30
ADAM_LR, ADAM_B1, ADAM_B2, ADAM_EPS, ADAM_WD, ADAM_STEP = 0.001, 0.9, 0.999, 1e-08, 0.01, 10

LANES = 128
QBLK = 128
KWIN = QBLK + 2 * HALF_SPAN
VMEM_LIMIT = 48 * 1024 * 1024
N_CHIPS = 4
HIGHEST = lax.Precision.HIGHEST


def _params(**kw):
    return pltpu.CompilerParams(vmem_limit_bytes=VMEM_LIMIT, **kw)


class _Rider:
    def __init__(self, operands, out_shapes, aliases, scratch, start, wait):
        self.operands, self.out_shapes, self.aliases = list(operands), list(out_shapes), dict(aliases)
        self.scratch, self.start, self.wait = list(scratch), start, wait


def _pallas(body, *, name, grid, in_specs, out_specs, out_shape, operands, scratch_shapes=(), aliases=None,
            rider=None):
    single = not isinstance(out_specs, (list, tuple))
    out_specs_l = [out_specs] if single else list(out_specs)
    out_shape_l = [out_shape] if single else list(out_shape)
    aliases = dict(aliases or {})
    if rider is None:
        res = pl.pallas_call(
            body, name=name, grid=grid, in_specs=list(in_specs), out_specs=out_specs_l, out_shape=out_shape_l,
            scratch_shapes=list(scratch_shapes), input_output_aliases=aliases, compiler_params=_params(),
        )(*operands)
        return res[0] if single else res
    n_in, n_rin = len(in_specs), len(rider.operands)
    n_out, n_rout = len(out_specs_l), len(rider.out_shapes)
    n_sc = len(scratch_shapes)

    def wrapped(*refs):
        main_in, r_in = refs[:n_in], refs[n_in:n_in + n_rin]
        o0 = n_in + n_rin
        main_out, r_out = refs[o0:o0 + n_out], refs[o0 + n_out:o0 + n_out + n_rout]
        s0 = o0 + n_out + n_rout
        main_sc, r_sc = refs[s0:s0 + n_sc], refs[s0 + n_sc:]
        ids = [pl.program_id(d) for d in range(len(grid))]
        first = functools.reduce(jnp.logical_and, [i == 0 for i in ids])
        last = functools.reduce(jnp.logical_and, [i == n - 1 for i, n in zip(ids, grid)])

        @pl.when(first)
        def _():
            rider.start(r_in, r_out, r_sc)

        body(*main_in, *main_out, *main_sc)

        @pl.when(last)
        def _():
            rider.wait(r_in, r_out, r_sc)

    for src, dst in rider.aliases.items():
        aliases[n_in + src] = n_out + dst
    res = pl.pallas_call(
        wrapped, name=name, grid=grid, in_specs=list(in_specs) + [ANY] * n_rin,
        out_specs=out_specs_l + [ANY] * n_rout, out_shape=out_shape_l + rider.out_shapes,
        scratch_shapes=list(scratch_shapes) + rider.scratch, input_output_aliases=aliases,
        compiler_params=_params(),
    )(*operands, *rider.operands)
    main = res[:n_out]
    return (main[0] if single else main), res[n_out:]


def _matmul(a, b, *, mode, tm, tn, tk, out_dtype, name, b_blocked=False,
            out_blocked=None, residual=None, rider=None):
    a_shape = a.shape
    if mode == "nn":
        m_dim, k_dim = a_shape
        n_dim = b.shape[0] * b.shape[2] if b_blocked else b.shape[1]
        rows, cols, red = m_dim, n_dim, k_dim
    elif mode == "nt":
        m_dim, n_dim = a_shape
        k_dim = b.shape[1] if b_blocked else b.shape[0]
        rows, cols, red = m_dim, k_dim, n_dim
    else:
        m_dim, k_dim = a_shape
        n_dim = b.shape[1]
        rows, cols, red = k_dim, n_dim, m_dim
    assert rows % tm == 0 and cols % tn == 0 and red % tk == 0, (name, rows, cols, red)
    ni, nj, nk = rows // tm, cols // tn, red // tk

    if mode == "nn":
        a_spec = pl.BlockSpec((tm, tk), lambda i, j, k: (i, k))
        if b_blocked:
            per = b.shape[2] // tn
            b_spec = pl.BlockSpec((None, tk, tn), lambda i, j, k: (j // per, k, j % per))
        else:
            b_spec = pl.BlockSpec((tk, tn), lambda i, j, k: (k, j))
        dims = (((1,), (0,)), ((), ()))
    elif mode == "nt":
        a_spec = pl.BlockSpec((tm, tk), lambda i, j, k: (i, k))
        if b_blocked:
            per = b.shape[2] // tk
            b_spec = pl.BlockSpec((None, tn, tk), lambda i, j, k: (k // per, j, k % per))
        else:
            b_spec = pl.BlockSpec((tn, tk), lambda i, j, k: (j, k))
        dims = (((1,), (1,)), ((), ()))
    else:
        a_spec = pl.BlockSpec((tk, tm), lambda i, j, k: (k, i))
        b_spec = pl.BlockSpec((tk, tn), lambda i, j, k: (k, j))
        dims = (((0,), (0,)), ((), ()))

    if out_blocked:
        per_o = (cols // out_blocked) // tn
        out_spec = pl.BlockSpec((None, tm, tn), lambda i, j, k: (j // per_o, i, j % per_o))
        out_shape = jax.ShapeDtypeStruct((out_blocked, rows, cols // out_blocked), out_dtype)
    else:
        out_spec = pl.BlockSpec((tm, tn), lambda i, j, k: (i, j))
        out_shape = jax.ShapeDtypeStruct((rows, cols), out_dtype)

    in_specs = [a_spec, b_spec]
    operands = [a, b]
    if residual is not None:
        in_specs.append(pl.BlockSpec((tm, tn), lambda i, j, k: (i, j)))
        operands.append(residual)
    has_res = residual is not None

    def body(*refs):
        a_ref, b_ref = refs[0], refs[1]
        res_ref = refs[2] if has_res else None
        o_ref = refs[3] if has_res else refs[2]
        prod = lax.dot_general(a_ref[...], b_ref[...], dims, preferred_element_type=F32)

        def finish(val):
            if has_res:
                val = val + res_ref[...]
            o_ref[...] = val.astype(out_dtype)

        if nk == 1:
            finish(prod)
        else:
            acc_ref = refs[-1]
            k = pl.program_id(2)

            @pl.when(k == 0)
            def _():
                acc_ref[...] = prod

            @pl.when(k > 0)
            def _():
                acc_ref[...] += prod

            @pl.when(k == nk - 1)
            def _():
                finish(acc_ref[...])

    scratch = [pltpu.VMEM((tm, tn), F32)] if nk > 1 else []
    return _pallas(body, name=name, grid=(ni, nj, nk), in_specs=in_specs, out_specs=out_spec,
                   out_shape=out_shape, operands=operands, scratch_shapes=scratch, rider=rider)


def _rmsnorm_fwd(x, w, name):
    s, d = x.shape
    tm = 256

    def body(x_ref, w_ref, o_ref):
        xv = x_ref[...]
        rstd = lax.rsqrt(jnp.mean(xv * xv, axis=-1, keepdims=True) + EPS)
        o_ref[...] = (xv * rstd * w_ref[...]).astype(BF16)

    return pl.pallas_call(
        body, name=name, grid=(s // tm,),
        in_specs=[pl.BlockSpec((tm, d), lambda i: (i, 0)), pl.BlockSpec((1, d), lambda i: (0, 0))],
        out_specs=pl.BlockSpec((tm, d), lambda i: (i, 0)),
        out_shape=jax.ShapeDtypeStruct((s, d), BF16), compiler_params=_params(),
    )(x, w)


def _rmsnorm_bwd(dh, x, w, dres, name):
    s, d = x.shape
    tm = 256

    def body(dh_ref, x_ref, w_ref, dres_ref, dx_ref, dxb_ref, dw_ref):
        xv = x_ref[...]
        rstd = lax.rsqrt(jnp.mean(xv * xv, axis=-1, keepdims=True) + EPS)
        xhat = xv * rstd
        dhv = dh_ref[...]
        g = dhv * w_ref[...]
        dx = rstd * (g - xhat * jnp.mean(g * xhat, axis=-1, keepdims=True)) + dres_ref[...]
        dx_ref[...] = dx
        dxb_ref[...] = dx.astype(BF16)
        part = jnp.sum(dhv * xhat, axis=0, keepdims=True)

        @pl.when(pl.program_id(0) == 0)
        def _():
            dw_ref[...] = part

        @pl.when(pl.program_id(0) > 0)
        def _():
            dw_ref[...] += part

    row = pl.BlockSpec((tm, d), lambda i: (i, 0))
    vec = pl.BlockSpec((1, d), lambda i: (0, 0))
    return pl.pallas_call(
        body, name=name, grid=(s // tm,), in_specs=[row, row, vec, row], out_specs=[row, row, vec],
        out_shape=[jax.ShapeDtypeStruct((s, d), F32), jax.ShapeDtypeStruct((s, d), BF16),
                   jax.ShapeDtypeStruct((1, d), F32)],
        compiler_params=_params(),
    )(dh, x, w, dres)


def _rope_consts():
    lane = np.arange(LANES)
    in_head = lane % HEAD_DIM
    inv_freq = ROPE_THETA ** (-jnp.arange(0, ROT_DIM, 2, dtype=F32) / ROT_DIM)
    invf = jnp.where(jnp.asarray(in_head < ROT_DIM), jnp.tile(inv_freq, LANES // (ROT_DIM // 2)), 0.0)
    m_a = np.where(in_head < ROT_DIM // 2, -1.0, 0.0).astype(np.float32)
    m_b = np.where((in_head >= ROT_DIM // 2) & (in_head < ROT_DIM), 1.0, 0.0).astype(np.float32)
    block_diag = (lane[:, None] // HEAD_DIM == lane[None, :] // HEAD_DIM).astype(np.float32)
    return (invf.reshape(1, LANES).astype(F32), jnp.asarray(m_a).reshape(1, LANES),
            jnp.asarray(m_b).reshape(1, LANES), jnp.asarray(block_diag))


def _head_sums(v, bd):
    return jnp.dot(v, bd, precision=HIGHEST, preferred_element_type=F32)


def _qk_fwd(proj, pos_col, qw2, kw2, consts, name):
    s = proj.shape[0]
    width = 3 * N_SLOT_HEADS * HEAD_DIM
    tm = 256
    invf, m_a, m_b, bd = consts
    scale = HEAD_DIM ** -0.5

    def body(q_ref, k_ref, pos_ref, qw_ref, kw_ref, invf_ref, ma_ref, mb_ref, bd_ref, qo_ref, ko_ref):
        ang = pos_ref[...].astype(F32) * invf_ref[...]
        cos = jnp.cos(ang)
        sin = jnp.sin(ang)
        s_a = sin * ma_ref[...]
        s_b = sin * mb_ref[...]
        bdv = bd_ref[...]
        for src, w_ref, dst, sc in ((q_ref, qw_ref, qo_ref, scale), (k_ref, kw_ref, ko_ref, 1.0)):
            for cb in range(width // LANES):
                cols = slice(cb * LANES, (cb + 1) * LANES)
                t = src[:, cols]
                rstd = lax.rsqrt(_head_sums(t * t, bdv) * (1.0 / HEAD_DIM) + EPS)
                y = t * rstd * w_ref[...]
                r = y * cos + pltpu.roll(y, LANES - 8, axis=1) * s_a + pltpu.roll(y, 8, axis=1) * s_b
                dst[:, cols] = r * sc if sc != 1.0 else r

    vec = pl.BlockSpec((1, LANES), lambda i: (0, 0))
    return pl.pallas_call(
        body, name=name, grid=(s // tm,),
        in_specs=[pl.BlockSpec((tm, width), lambda i: (i, 0)), pl.BlockSpec((tm, width), lambda i: (i, 1)),
                  pl.BlockSpec((tm, 1), lambda i: (i, 0)), vec, vec, vec, vec, vec,
                  pl.BlockSpec((LANES, LANES), lambda i: (0, 0))],
        out_specs=[pl.BlockSpec((tm, width), lambda i: (i, 0))] * 2,
        out_shape=[jax.ShapeDtypeStruct((s, width), F32)] * 2, compiler_params=_params(),
    )(proj, proj, pos_col, qw2, kw2, invf, m_a, m_b, bd)


def _qk_bwd(dproj, dqn, dkn, dv, da, db, proj, pos_col, qw2, kw2, consts, name, rider=None):
    s = proj.shape[0]
    width = 3 * N_SLOT_HEADS * HEAD_DIM
    ch = da.shape[1]
    out_w = 3 * width + 2 * ch
    tm = 256
    invf, m_a, m_b, bd = consts
    scale = HEAD_DIM ** -0.5

    def body(dproj_in, dq_ref, dk_ref, dv_ref, da_ref, db_ref, q_ref, k_ref, pos_ref, qw_ref, kw_ref,
             invf_ref, ma_ref, mb_ref, bd_ref, out_ref, dqw_ref, dkw_ref):
        del dproj_in
        ang = pos_ref[...].astype(F32) * invf_ref[...]
        cos = jnp.cos(ang)
        sin = jnp.sin(ang)
        s_a = sin * ma_ref[...]
        s_b = sin * mb_ref[...]
        bdv = bd_ref[...]
        first = pl.program_id(0) == 0
        for src, dsrc, w_ref, col0, dw_ref, sc in ((q_ref, dq_ref, qw_ref, 0, dqw_ref, scale),
                                                   (k_ref, dk_ref, kw_ref, width, dkw_ref, 1.0)):
            dw_acc = jnp.zeros((1, LANES), F32)
            for cb in range(width // LANES):
                cols = slice(cb * LANES, (cb + 1) * LANES)
                t = src[:, cols]
                dr = dsrc[:, cols]
                if sc != 1.0:
                    dr = dr * sc
                dy = dr * cos + pltpu.roll(dr * s_a, 8, axis=1) + pltpu.roll(dr * s_b, LANES - 8, axis=1)
                rstd = lax.rsqrt(_head_sums(t * t, bdv) * (1.0 / HEAD_DIM) + EPS)
                xhat = t * rstd
                g = dy * w_ref[...]
                dt = rstd * (g - xhat * (_head_sums(g * xhat, bdv) * (1.0 / HEAD_DIM)))
                out_ref[:, col0 + cb * LANES: col0 + (cb + 1) * LANES] = dt.astype(BF16)
                dw_acc = dw_acc + jnp.sum(dy * xhat, axis=0, keepdims=True)
            dw_acc = dw_acc + pltpu.roll(dw_acc, HEAD_DIM, axis=1)

            @pl.when(first)
            def _(dw_ref=dw_ref, dw_acc=dw_acc):
                dw_ref[...] = dw_acc

            @pl.when(jnp.logical_not(first))
            def _(dw_ref=dw_ref, dw_acc=dw_acc):
                dw_ref[...] += dw_acc
        out_ref[:, 2 * width: 3 * width] = dv_ref[...].astype(BF16)
        out_ref[:, 3 * width: 3 * width + ch] = da_ref[...]
        out_ref[:, 3 * width + ch: out_w] = db_ref[...]

    vec = pl.BlockSpec((1, LANES), lambda i: (0, 0))
    blk = lambda c: pl.BlockSpec((tm, width), lambda i: (i, c))
    cblk = pl.BlockSpec((tm, ch), lambda i: (i, 0))
    return _pallas(
        body, name=name, grid=(s // tm,),
        in_specs=[ANY, blk(0), blk(0), blk(0), cblk, cblk, blk(0), blk(1),
                  pl.BlockSpec((tm, 1), lambda i: (i, 0)), vec, vec, vec, vec, vec,
                  pl.BlockSpec((LANES, LANES), lambda i: (0, 0))],
        out_specs=[pl.BlockSpec((tm, out_w), lambda i: (i, 0)), vec, vec],
        out_shape=[jax.ShapeDtypeStruct(dproj.shape, BF16)] + [jax.ShapeDtypeStruct((1, LANES), F32)] * 2,
        operands=[dproj, dqn, dkn, dv, da, db, proj, proj, pos_col, qw2, kw2, invf, m_a, m_b, bd],
        aliases={0: 0}, rider=rider)


def _row_chunks(n_rows, fn, chunk=256):
    def step(i, c):
        fn(pl.ds(pl.multiple_of(i * chunk, chunk), chunk))
        return c
    lax.fori_loop(0, n_rows // chunk, step, 0)


def _to_residue_major(dst, src, s, d, dst_off=0, cast=None):
    seq = s // d
    for r in range(d):
        v = src[...] if d == 1 else src[pl.ds(r, seq, stride=d), :]
        dst[dst_off + r * seq: dst_off + (r + 1) * seq, :] = v if cast is None else v.astype(cast)


def _from_residue_major(dst, src, s, d, src_off=0):
    seq = s // d
    for r in range(d):
        v = src[src_off + r * seq: src_off + (r + 1) * seq, :]
        if d == 1:
            dst[...] = v
        else:
            dst[pl.ds(r, seq, stride=d), :] = v


def _band_bias():
    qi = lax.broadcasted_iota(jnp.int32, (QBLK, KWIN), 0)
    kj = lax.broadcasted_iota(jnp.int32, (QBLK, KWIN), 1)
    return jnp.where(jnp.abs(kj - HALF_SPAN - qi) <= HALF_SPAN, 0.0, NEG_INF).astype(F32)


def _range_bias(base, seq):
    kj = lax.broadcasted_iota(jnp.int32, (1, KWIN), 1)
    lo = (base & -seq) - base + HALF_SPAN
    return jnp.where((kj >= lo) & (kj < lo + seq), 0.0, NEG_INF).astype(F32)


def _block_base(b):
    return b * QBLK if isinstance(b, int) else pl.multiple_of(b * QBLK, QBLK)


def _attn_fwd(qn, kn, proj, name, rider=None):
    s = qn.shape[0]
    n_pairs = N_SLOT_HEADS * HEAD_DIM // LANES
    v_col0 = 2 * qn.shape[1] // LANES
    nt_dims = (((1,), (1,)), ((), ()))

    def body(q_ref, k_ref, v_ref, attn_ref, lse_ref, q_rm, k_rm, v_rm, acc_rm, m_rm, l_rm,
             acc_p, m_p, l_p, m_run, l_run, acc_run, band, s_buf, m_buf):
        g = pl.program_id(1)
        zpad = jnp.zeros((HALF_SPAN, LANES), BF16)
        k_rm[0:HALF_SPAN, :] = zpad
        k_rm[s + HALF_SPAN: s + 2 * HALF_SPAN, :] = zpad
        v_rm[0:HALF_SPAN, 0:LANES] = zpad
        v_rm[s + HALF_SPAN: s + 2 * HALF_SPAN, 0:LANES] = zpad

        def ones_rows(rows):
            v_rm[pl.ds(rows.start, rows.size), LANES:2 * LANES] = jnp.ones((rows.size, LANES), BF16)

        _row_chunks(s + 2 * HALF_SPAN, ones_rows, chunk=2 * HALF_SPAN)
        band[...] = _band_bias()
        lane = lax.broadcasted_iota(jnp.int32, (QBLK, LANES), 1)
        low = lane < HEAD_DIM
        n_blk = s // QBLK

        for gi, d in enumerate(DILATIONS):
            @pl.when(g == gi)
            def _(gi=gi, d=d):
                seq = s // d
                _to_residue_major(q_rm, q_ref, s, d, cast=BF16)
                _to_residue_major(k_rm, k_ref, s, d, dst_off=HALF_SPAN, cast=BF16)
                _to_residue_major(v_rm.at[:, 0:LANES], v_ref, s, d, dst_off=HALF_SPAN, cast=BF16)

                def scores(b, slot):
                    base = _block_base(b)
                    q = q_rm[pl.ds(base, QBLK), :]
                    zero = jnp.zeros_like(q)
                    q2 = jnp.concatenate([jnp.where(low, q, zero), jnp.where(low, zero, q)], axis=0)
                    sc = lax.dot_general(q2, k_rm[pl.ds(base, KWIN), :], nt_dims, preferred_element_type=F32)
                    bias = band[...] + _range_bias(base, seq)
                    for hh in range(2):
                        rows = slice(hh * QBLK, (hh + 1) * QBLK)
                        sh = sc[rows, :] + bias
                        s_buf[slot, rows, :] = sh
                        m_buf[slot, rows, :] = jnp.broadcast_to(jnp.max(sh, axis=-1, keepdims=True), (QBLK, LANES))

                def outputs(b, slot):
                    base = _block_base(b)
                    sv = s_buf[slot]
                    mb = m_buf[slot]
                    p = jnp.exp(jnp.concatenate([sv[:, 0:LANES] - mb, sv[:, LANES:2 * LANES] - mb], axis=1))
                    pv = jnp.dot(p.astype(BF16), v_rm[pl.ds(base, KWIN), :], preferred_element_type=F32)
                    rows = pl.ds(base, QBLK)
                    acc_rm[rows, :] = jnp.where(low, pv[0:QBLK, 0:LANES], pv[QBLK:2 * QBLK, 0:LANES])
                    l_rm[rows, :] = jnp.where(low, pv[0:QBLK, LANES:2 * LANES], pv[QBLK:2 * QBLK, LANES:2 * LANES])
                    m_rm[rows, :] = jnp.where(low, mb[0:QBLK, :], mb[QBLK:2 * QBLK, :])

                scores(0, 0)

                def pair(i, carry):
                    b = 2 * i
                    outputs(b, 0)
                    scores(b + 1, 1)
                    outputs(b + 1, 1)
                    scores(b + 2, 0)
                    return carry

                lax.fori_loop(0, n_blk // 2 - 1, pair, 0)
                outputs(n_blk - 2, 0)
                scores(n_blk - 1, 1)
                outputs(n_blk - 1, 1)
                if d == 1:
                    src = (acc_rm, m_rm, l_rm)
                else:
                    for dst_, src_ in ((acc_p, acc_rm), (m_p, m_rm), (l_p, l_rm)):
                        _from_residue_major(dst_, src_, s, d)
                    src = (acc_p, m_p, l_p)

                def combine(rows):
                    a_g, m_g, l_g = src[0][rows, :], src[1][rows, :], src[2][rows, :]
                    if gi == 0:
                        m_new, l_new, a_new = m_g, l_g, a_g
                    else:
                        m_old = m_run[rows, :]
                        m_new = jnp.maximum(m_old, m_g)
                        w_old = jnp.exp(m_old - m_new)
                        w_g = jnp.exp(m_g - m_new)
                        l_new = l_run[rows, :] * w_old + l_g * w_g
                        a_new = acc_run[rows, :] * w_old + a_g * w_g
                    if gi == len(DILATIONS) - 1:
                        attn_ref[rows, :] = a_new / l_new
                        lse_ref[rows, :] = m_new + jnp.log(l_new)
                    else:
                        m_run[rows, :] = m_new
                        l_run[rows, :] = l_new
                        acc_run[rows, :] = a_new

                _row_chunks(s, combine)

    qk_spec = pl.BlockSpec((s, LANES), lambda hp, g: (0, g * n_pairs + hp))
    v_spec = pl.BlockSpec((s, LANES), lambda hp, g: (0, v_col0 + g * n_pairs + hp))
    o_spec = pl.BlockSpec((s, LANES), lambda hp, g: (0, hp))
    f32buf = pltpu.VMEM((s, LANES), F32)
    return _pallas(
        body, name=name, grid=(n_pairs, len(DILATIONS)), in_specs=[qk_spec, qk_spec, v_spec],
        out_specs=[o_spec, o_spec],
        out_shape=[jax.ShapeDtypeStruct((s, n_pairs * LANES), F32)] * 2,
        operands=[qn, kn, proj],
        scratch_shapes=[pltpu.VMEM((s, LANES), BF16), pltpu.VMEM((s + 2 * HALF_SPAN, LANES), BF16),
                        pltpu.VMEM((s + 2 * HALF_SPAN, 2 * LANES), BF16)] + [f32buf] * 9
        + [pltpu.VMEM((QBLK, KWIN), F32), pltpu.VMEM((2, 2 * QBLK, KWIN), F32),
           pltpu.VMEM((2, 2 * QBLK, LANES), F32)],
        rider=rider)


def _attn_bwd(qn, kn, proj, dattn, attn, lse, bd, name, rider=None):
    s = qn.shape[0]
    n_pairs = N_SLOT_HEADS * HEAD_DIM // LANES
    v_col0 = 2 * qn.shape[1] // LANES
    nt_dims = (((1,), (1,)), ((), ()))
    tn_dims = (((0,), (0,)), ((), ()))
    spad = s + 2 * HALF_SPAN

    def body(q_ref, k_ref, v_ref, do_ref, o_ref, lse_ref, bd_ref, dq_ref, dk_ref, dv_ref,
             q_rm, k_rm, v_rm, do_rm, lse0_rm, lse1_rm, dd0_rm, dd1_rm, dq_rm, dk_rm, dv_rm,
             lse0_p, lse1_p, dd0_p, dd1_p, band, p_buf, ds_buf):
        g = pl.program_id(1)
        zpad = jnp.zeros((HALF_SPAN, LANES), BF16)
        for buf in (k_rm, v_rm):
            buf[0:HALF_SPAN, :] = zpad
            buf[s + HALF_SPAN: spad, :] = zpad
        zf = jnp.zeros((HALF_SPAN, LANES), F32)
        for buf in (dk_rm, dv_rm):
            buf[0:HALF_SPAN, :] = zf
            buf[s + HALF_SPAN: spad, :] = zf
        band[...] = _band_bias()

        def clear(rows):
            z = jnp.zeros((rows.size, LANES), F32)
            dk_rm[pl.ds(rows.start + HALF_SPAN, rows.size), :] = z
            dv_rm[pl.ds(rows.start + HALF_SPAN, rows.size), :] = z

        _row_chunks(s, clear)

        def prepare(rows):
            lo = lax.broadcasted_iota(jnp.int32, (rows.size, LANES), 1) < HEAD_DIM
            dsum = _head_sums(do_ref[rows, :] * o_ref[rows, :], bd_ref[...])
            dswap = pltpu.roll(dsum, HEAD_DIM, axis=1)
            dd0_p[rows, :] = jnp.where(lo, dsum, dswap)
            dd1_p[rows, :] = jnp.where(lo, dswap, dsum)
            lv = lse_ref[rows, :]
            lswap = pltpu.roll(lv, HEAD_DIM, axis=1)
            lse0_p[rows, :] = jnp.where(lo, lv, lswap)
            lse1_p[rows, :] = jnp.where(lo, lswap, lv)

        @pl.when(g == 0)
        def _():
            _row_chunks(s, prepare)
        lane = lax.broadcasted_iota(jnp.int32, (QBLK, LANES), 1)
        low = lane < HEAD_DIM
        n_blk = s // QBLK

        def stacked(ref, rows):
            val = ref[rows, :]
            zero = jnp.zeros_like(val)
            return jnp.concatenate([jnp.where(low, val, zero), jnp.where(low, zero, val)], axis=0)

        for gi, d in enumerate(DILATIONS):
            @pl.when(g == gi)
            def _(d=d):
                seq = s // d
                _to_residue_major(q_rm, q_ref, s, d, cast=BF16)
                _to_residue_major(k_rm, k_ref, s, d, dst_off=HALF_SPAN, cast=BF16)
                _to_residue_major(v_rm, v_ref, s, d, dst_off=HALF_SPAN, cast=BF16)
                _to_residue_major(do_rm, do_ref, s, d, cast=BF16)
                for dst_, src_ in ((lse0_rm, lse0_p), (lse1_rm, lse1_p), (dd0_rm, dd0_p), (dd1_rm, dd1_p)):
                    _to_residue_major(dst_, src_, s, d)

                def scores(b, slot):
                    base = _block_base(b)
                    rows = pl.ds(base, QBLK)
                    win = pl.ds(base, KWIN)
                    sc = lax.dot_general(stacked(q_rm, rows), k_rm[win, :], nt_dims, preferred_element_type=F32)
                    dp = lax.dot_general(stacked(do_rm, rows), v_rm[win, :], nt_dims, preferred_element_type=F32)
                    bias = band[...] + _range_bias(base, seq)
                    for hh, (lse_r, dd_r) in enumerate(((lse0_rm, dd0_rm), (lse1_rm, dd1_rm))):
                        r = slice(hh * QBLK, (hh + 1) * QBLK)
                        lse_h = lse_r[rows, :]
                        dd_h = dd_r[rows, :]
                        sh = sc[r, :] + bias
                        p = jnp.exp(jnp.concatenate([sh[:, 0:LANES] - lse_h, sh[:, LANES:KWIN] - lse_h], axis=1))
                        dph = dp[r, :]
                        ds = p * jnp.concatenate([dph[:, 0:LANES] - dd_h, dph[:, LANES:KWIN] - dd_h], axis=1)
                        p_buf[slot, r, :] = p.astype(BF16)
                        ds_buf[slot, r, :] = ds.astype(BF16)

                def grads(b, slot):
                    base = _block_base(b)
                    rows = pl.ds(base, QBLK)
                    win = pl.ds(base, KWIN)
                    p = p_buf[slot]
                    ds = ds_buf[slot]
                    dq2 = jnp.dot(ds, k_rm[win, :], preferred_element_type=F32)
                    dq_rm[rows, :] = jnp.where(low, dq2[0:QBLK, :], dq2[QBLK:2 * QBLK, :])
                    dk_rm[win, :] += lax.dot_general(ds, stacked(q_rm, rows), tn_dims, preferred_element_type=F32)
                    dv_rm[win, :] += lax.dot_general(p, stacked(do_rm, rows), tn_dims, preferred_element_type=F32)

                scores(0, 0)

                def pair(i, carry):
                    b = 2 * i
                    grads(b, 0)
                    scores(b + 1, 1)
                    grads(b + 1, 1)
                    scores(b + 2, 0)
                    return carry

                lax.fori_loop(0, n_blk // 2 - 1, pair, 0)
                grads(n_blk - 2, 0)
                scores(n_blk - 1, 1)
                grads(n_blk - 1, 1)
                _from_residue_major(dq_ref, dq_rm, s, d)
                _from_residue_major(dk_ref, dk_rm, s, d, src_off=HALF_SPAN)
                _from_residue_major(dv_ref, dv_rm, s, d, src_off=HALF_SPAN)

    qk_spec = pl.BlockSpec((s, LANES), lambda hp, g: (0, g * n_pairs + hp))
    v_spec = pl.BlockSpec((s, LANES), lambda hp, g: (0, v_col0 + g * n_pairs + hp))
    o_spec = pl.BlockSpec((s, LANES), lambda hp, g: (0, hp))
    width = qn.shape[1]
    f32buf = pltpu.VMEM((s, LANES), F32)
    f32pad = pltpu.VMEM((spad, LANES), F32)
    return _pallas(
        body, name=name, grid=(n_pairs, len(DILATIONS)),
        in_specs=[qk_spec, qk_spec, v_spec, o_spec, o_spec, o_spec,
                  pl.BlockSpec((LANES, LANES), lambda hp, g: (0, 0))],
        out_specs=[qk_spec, qk_spec, qk_spec],
        out_shape=[jax.ShapeDtypeStruct((s, width), F32)] * 3,
        operands=[qn, kn, proj, dattn, attn, lse, bd],
        scratch_shapes=[pltpu.VMEM((s, LANES), BF16), pltpu.VMEM((spad, LANES), BF16),
                        pltpu.VMEM((spad, LANES), BF16), pltpu.VMEM((s, LANES), BF16),
                        f32buf, f32buf, f32buf, f32buf, f32buf, f32pad, f32pad,
                        f32buf, f32buf, f32buf, f32buf, pltpu.VMEM((QBLK, KWIN), F32),
                        pltpu.VMEM((2, 2 * QBLK, KWIN), BF16), pltpu.VMEM((2, 2 * QBLK, KWIN), BF16)],
        rider=rider)


CONV_PAD = 16


def _conv_fwd(proj, conv_w, conv_b, col0, name, rider=None):
    s = proj.shape[0]
    ch = conv_w.shape[1]
    nblk = ch // LANES
    a0 = col0 // LANES
    tr = 256
    shift = CONV_PAD - (CONV_WIDTH - 1) // 2

    def body(a_ref, b_ref, w_ref, bias_ref, u0_ref, uc_ref, pad):
        z = jnp.zeros((CONV_PAD, LANES), F32)
        pad[0:CONV_PAD, :] = z
        pad[s + CONV_PAD: s + 2 * CONV_PAD, :] = z

        def glu(rows):
            u0 = a_ref[rows, :] * jax.nn.sigmoid(b_ref[rows, :])
            u0_ref[rows, :] = u0
            pad[pl.ds(rows.start + CONV_PAD, rows.size), :] = u0

        _row_chunks(s, glu)
        for t in range(0, s, tr):
            acc = jnp.broadcast_to(bias_ref[...], (tr, LANES))
            for k in range(CONV_WIDTH):
                acc = acc + w_ref[k:k + 1, :] * pad[t + k + shift: t + k + shift + tr, :]
            uc_ref[t:t + tr, :] = acc

    return _pallas(
        body, name=name, grid=(nblk,),
        in_specs=[pl.BlockSpec((s, LANES), lambda c: (0, a0 + c)),
                  pl.BlockSpec((s, LANES), lambda c: (0, a0 + nblk + c)),
                  pl.BlockSpec((CONV_WIDTH, LANES), lambda c: (0, c)),
                  pl.BlockSpec((1, LANES), lambda c: (0, c))],
        out_specs=[pl.BlockSpec((s, LANES), lambda c: (0, c))] * 2,
        out_shape=[jax.ShapeDtypeStruct((s, ch), F32)] * 2, operands=[proj, proj, conv_w, conv_b],
        scratch_shapes=[pltpu.VMEM((s + 2 * CONV_PAD, LANES), F32)], rider=rider)


def _ln_silu_fwd(uc, ln_w, ln_b, name):
    s, ch = uc.shape
    tm = 256

    def body(u_ref, w_ref, b_ref, o_ref):
        u = u_ref[...]
        mu = jnp.mean(u, axis=-1, keepdims=True)
        xc = u - mu
        rstd = lax.rsqrt(jnp.mean(xc * xc, axis=-1, keepdims=True) + EPS)
        z = xc * rstd * w_ref[...] + b_ref[...]
        o_ref[...] = (z * jax.nn.sigmoid(z)).astype(BF16)

    row = pl.BlockSpec((tm, ch), lambda i: (i, 0))
    vec = pl.BlockSpec((1, ch), lambda i: (0, 0))
    return pl.pallas_call(
        body, name=name, grid=(s // tm,), in_specs=[row, vec, vec], out_specs=row,
        out_shape=jax.ShapeDtypeStruct((s, ch), BF16), compiler_params=_params(),
    )(uc, ln_w, ln_b)


def _ln_silu_bwd(du3, uc, ln_w, ln_b, name):
    s, ch = uc.shape
    tm = 256

    def body(d_ref, u_ref, w_ref, b_ref, du_ref, dw_ref, db_ref):
        u = u_ref[...]
        mu = jnp.mean(u, axis=-1, keepdims=True)
        xc = u - mu
        rstd = lax.rsqrt(jnp.mean(xc * xc, axis=-1, keepdims=True) + EPS)
        xhat = xc * rstd
        z = xhat * w_ref[...] + b_ref[...]
        sg = jax.nn.sigmoid(z)
        dz = d_ref[...] * (sg * (1.0 + z * (1.0 - sg)))
        dxh = dz * w_ref[...]
        du_ref[...] = rstd * (dxh - jnp.mean(dxh, axis=-1, keepdims=True)
                              - xhat * jnp.mean(dxh * xhat, axis=-1, keepdims=True))
        pw = jnp.sum(dz * xhat, axis=0, keepdims=True)
        pb = jnp.sum(dz, axis=0, keepdims=True)
        first = pl.program_id(0) == 0

        @pl.when(first)
        def _():
            dw_ref[...] = pw
            db_ref[...] = pb

        @pl.when(jnp.logical_not(first))
        def _():
            dw_ref[...] += pw
            db_ref[...] += pb

    row = pl.BlockSpec((tm, ch), lambda i: (i, 0))
    vec = pl.BlockSpec((1, ch), lambda i: (0, 0))
    return pl.pallas_call(
        body, name=name, grid=(s // tm,), in_specs=[row, row, vec, vec], out_specs=[row, vec, vec],
        out_shape=[jax.ShapeDtypeStruct((s, ch), F32), jax.ShapeDtypeStruct((1, ch), F32),
                   jax.ShapeDtypeStruct((1, ch), F32)],
        compiler_params=_params(),
    )(du3, uc, ln_w, ln_b)


def _conv_bwd(duc, u0, proj, conv_w, col0, name, rider=None):
    s = proj.shape[0]
    ch = conv_w.shape[1]
    nblk = ch // LANES
    a0 = col0 // LANES
    tr = 256
    half = (CONV_WIDTH - 1) // 2
    shift = CONV_PAD - half

    def body(duc_ref, u0_ref, a_ref, b_ref, w_ref, da_ref, db_ref, dw_ref, dbias_ref, pad_d, pad_u):
        z = jnp.zeros((CONV_PAD, LANES), F32)
        for buf in (pad_d, pad_u):
            buf[0:CONV_PAD, :] = z
            buf[s + CONV_PAD: s + 2 * CONV_PAD, :] = z

        def fill(rows):
            dst = pl.ds(rows.start + CONV_PAD, rows.size)
            pad_d[dst, :] = duc_ref[rows, :]
            pad_u[dst, :] = u0_ref[rows, :]

        _row_chunks(s, fill)
        dw_acc = [jnp.zeros((8, LANES), F32) for _ in range(CONV_WIDTH)]
        dbias_acc = jnp.zeros((8, LANES), F32)
        for t in range(0, s, tr):
            d_t = duc_ref[t:t + tr, :]
            dbias_acc = dbias_acc + jnp.sum(d_t.reshape(tr // 8, 8, LANES), axis=0)
            du0 = jnp.zeros((tr, LANES), F32)
            for k in range(CONV_WIDTH):
                du0 = du0 + w_ref[k:k + 1, :] * pad_d[t - k + half + CONV_PAD: t - k + half + CONV_PAD + tr, :]
                prod = d_t * pad_u[t + k + shift: t + k + shift + tr, :]
                dw_acc[k] = dw_acc[k] + jnp.sum(prod.reshape(tr // 8, 8, LANES), axis=0)
            av = a_ref[t:t + tr, :]
            sg = jax.nn.sigmoid(b_ref[t:t + tr, :])
            da_ref[t:t + tr, :] = (du0 * sg).astype(BF16)
            db_ref[t:t + tr, :] = (du0 * av * sg * (1.0 - sg)).astype(BF16)
        for k in range(CONV_WIDTH):
            dw_ref[k:k + 1, :] = jnp.sum(dw_acc[k], axis=0, keepdims=True)
        dbias_ref[...] = jnp.sum(dbias_acc, axis=0, keepdims=True)

    col = lambda off: pl.BlockSpec((s, LANES), lambda c: (0, off + c))
    return _pallas(
        body, name=name, grid=(nblk,),
        in_specs=[col(0), col(0), col(a0), col(a0 + nblk),
                  pl.BlockSpec((CONV_WIDTH, LANES), lambda c: (0, c))],
        out_specs=[col(0), col(0), pl.BlockSpec((CONV_WIDTH, LANES), lambda c: (0, c)),
                   pl.BlockSpec((1, LANES), lambda c: (0, c))],
        out_shape=[jax.ShapeDtypeStruct((s, ch), BF16)] * 2
        + [jax.ShapeDtypeStruct((CONV_WIDTH, ch), F32), jax.ShapeDtypeStruct((1, ch), F32)],
        operands=[duc, u0, proj, proj, conv_w],
        scratch_shapes=[pltpu.VMEM((s + 2 * CONV_PAD, LANES), F32)] * 2, rider=rider)


GATE_BLK = 512


def _gate_fwd(proj, bg, y_a, y_b, col0, name):
    s, d = y_a.shape
    tm = 256
    g0 = col0 // GATE_BLK
    nb = d // GATE_BLK

    def body(ga_ref, gb_ref, ba_ref, bb_ref, ya_ref, yb_ref, o_ref):
        g_a = jax.nn.sigmoid(ga_ref[...] + ba_ref[...])
        g_b = jax.nn.sigmoid(gb_ref[...] + bb_ref[...])
        o_ref[...] = (g_a * ya_ref[...] + g_b * yb_ref[...]).astype(BF16)

    act = pl.BlockSpec((tm, GATE_BLK), lambda i, j: (i, j))
    return pl.pallas_call(
        body, name=name, grid=(s // tm, nb),
        in_specs=[pl.BlockSpec((tm, GATE_BLK), lambda i, j: (i, g0 + j)),
                  pl.BlockSpec((tm, GATE_BLK), lambda i, j: (i, g0 + nb + j)),
                  pl.BlockSpec((None, 1, GATE_BLK), lambda i, j: (0, 0, j)),
                  pl.BlockSpec((None, 1, GATE_BLK), lambda i, j: (1, 0, j)), act, act],
        out_specs=act, out_shape=jax.ShapeDtypeStruct((s, d), BF16), compiler_params=_params(),
    )(proj, proj, bg, bg, y_a, y_b)


def _gate_bwd(d_mixed, proj, bg, y_a, y_b, col0, name):
    s, d = y_a.shape
    tm = 256
    g0 = col0 // GATE_BLK
    nb = d // GATE_BLK

    def body(dm_ref, gl_ref, b_ref, y_ref, dgl_ref, dy_ref, db_ref):
        gate = jax.nn.sigmoid(gl_ref[...] + b_ref[...])
        dm = dm_ref[...]
        dy_ref[...] = (dm * gate).astype(BF16)
        dgl = dm * y_ref[...] * gate * (1.0 - gate)
        dgl_ref[...] = dgl.astype(BF16)
        part = jnp.sum(dgl, axis=0, keepdims=True)
        first = pl.program_id(1) == 0

        @pl.when(first)
        def _():
            db_ref[...] = part

        @pl.when(jnp.logical_not(first))
        def _():
            db_ref[...] += part

    def call(y, branch, dproj_prev):
        def wrapped(*refs):
            if dproj_prev is not None:
                refs = refs[1:]
            body(*refs)

        ins = [pl.BlockSpec((tm, GATE_BLK), lambda j, i: (i, j)),
               pl.BlockSpec((tm, GATE_BLK), lambda j, i: (i, g0 + branch * nb + j)),
               pl.BlockSpec((None, 1, GATE_BLK), lambda j, i: (branch, 0, j)),
               pl.BlockSpec((tm, GATE_BLK), lambda j, i: (i, j))]
        ops = [d_mixed, proj, bg, y]
        alias = {}
        if dproj_prev is not None:
            ins = [ANY] + ins
            ops = [dproj_prev] + ops
            alias = {0: 0}
        return pl.pallas_call(
            wrapped, name=f"{name}_{branch}", grid=(nb, s // tm), in_specs=ins,
            out_specs=[pl.BlockSpec((tm, GATE_BLK), lambda j, i: (i, g0 + branch * nb + j)),
                       pl.BlockSpec((tm, GATE_BLK), lambda j, i: (i, j)),
                       pl.BlockSpec((1, GATE_BLK), lambda j, i: (0, j))],
            out_shape=[jax.ShapeDtypeStruct((s, proj.shape[1]), BF16), jax.ShapeDtypeStruct((s, d), BF16),
                       jax.ShapeDtypeStruct((1, d), F32)],
            input_output_aliases=alias, compiler_params=_params(),
        )(*ops)

    dproj, dy_a, db_a = call(y_a, 0, None)
    dproj, dy_b, db_b = call(y_b, 1, dproj)
    return dproj, dy_a, dy_b, db_a, db_b


def _swiglu_fwd(gu, name):
    s, w2 = gu.shape
    ff = w2 // 2
    tm = 256
    tn = ff // 2

    def body(g_ref, u_ref, o_ref):
        gt = g_ref[...]
        o_ref[...] = (gt * jax.nn.sigmoid(gt) * u_ref[...]).astype(BF16)

    return pl.pallas_call(
        body, name=name, grid=(s // tm, 2),
        in_specs=[pl.BlockSpec((tm, tn), lambda i, j: (i, j)), pl.BlockSpec((tm, tn), lambda i, j: (i, 2 + j))],
        out_specs=pl.BlockSpec((tm, tn), lambda i, j: (i, j)),
        out_shape=jax.ShapeDtypeStruct((s, ff), BF16), compiler_params=_params(),
    )(gu, gu)


def _swiglu_bwd(gu, d_act, name):
    s, w2 = gu.shape
    ff = w2 // 2
    tm = 256
    tn = ff // 2

    def body(g_ref, u_ref, d_ref, o_ref):
        gt = g_ref[...]
        sg = jax.nn.sigmoid(gt)
        dv = d_ref[...]
        is_gate = pl.program_id(1) < 2

        @pl.when(is_gate)
        def _():
            o_ref[...] = (dv * u_ref[...] * (sg * (1.0 + gt * (1.0 - sg)))).astype(BF16)

        @pl.when(jnp.logical_not(is_gate))
        def _():
            o_ref[...] = (dv * gt * sg).astype(BF16)

    return pl.pallas_call(
        body, name=name, grid=(s // tm, 4),
        in_specs=[pl.BlockSpec((tm, tn), lambda i, j: (i, j % 2)),
                  pl.BlockSpec((tm, tn), lambda i, j: (i, 2 + j % 2)),
                  pl.BlockSpec((tm, tn), lambda i, j: (i, j % 2))],
        out_specs=pl.BlockSpec((tm, tn), lambda i, j: (i, j)),
        out_shape=jax.ShapeDtypeStruct((s, w2), BF16), compiler_params=_params(),
    )(gu, gu, d_act)


def _loss_fwd_bwd(y, target, name):
    s, d = y.shape
    tm = 256

    def body(y_ref, t_ref, dy_ref, dyb_ref, loss_ref, acc):
        diff = y_ref[...] - t_ref[...]
        dy = diff * (1.0 / d)
        dy_ref[...] = dy
        dyb_ref[...] = dy.astype(BF16)
        part = jnp.sum((diff * diff).reshape(tm // 8, 8, d), axis=0)
        i = pl.program_id(0)

        @pl.when(i == 0)
        def _():
            acc[...] = part

        @pl.when(i > 0)
        def _():
            acc[...] += part

        @pl.when(i == pl.num_programs(0) - 1)
        def _():
            loss_ref[...] = (0.5 / d) * jnp.sum(jnp.sum(acc[...], axis=1, keepdims=True), axis=0, keepdims=True)

    row = pl.BlockSpec((tm, d), lambda i: (i, 0))
    return pl.pallas_call(
        body, name=name, grid=(s // tm,), in_specs=[row, row],
        out_specs=[row, row, pl.BlockSpec((1, 1), lambda i: (0, 0))],
        out_shape=[jax.ShapeDtypeStruct((s, d), F32), jax.ShapeDtypeStruct((s, d), BF16),
                   jax.ShapeDtypeStruct((1, 1), F32)],
        scratch_shapes=[pltpu.VMEM((8, d), F32)], compiler_params=_params(),
    )(y, target)


LATE_GATHER = ("w_o_attn", "w_pw_conv", "w_out", "w_ffn_in", "w_ffn_out")
EARLY_REDUCE = LATE_GATHER


def _blocks_by_half(g):
    if g.ndim == 2:
        g = g.reshape(N_CHIPS, g.shape[0] // N_CHIPS, g.shape[1])
    return g.reshape(N_CHIPS, 2, g.shape[1] // 2, g.shape[2])


def _forward_backward(x, pos_col, target, wts, late_bufs, pos_arr):
    wts = dict(wts)
    consts = _rope_consts()
    bd = consts[3]
    qw2 = jnp.tile(wts["q_norm_w"], (1, LANES // HEAD_DIM))
    kw2 = jnp.tile(wts["k_norm_w"], (1, LANES // HEAD_DIM))
    qkv_w = 3 * N_SLOT_HEADS * HEAD_DIM
    conv_col0 = 3 * qkv_w
    ch = wts["conv_w"].shape[1]
    gate_col0 = conv_col0 + 2 * ch

    h = _rmsnorm_fwd(x, wts["norm1_w"], "rms1_fwd")
    proj = _matmul(h, wts["w_in"], mode="nn", tm=1024, tn=640, tk=1024, out_dtype=F32, name="mm_proj",
                   b_blocked=True)
    qn, kn = _qk_fwd(proj, pos_col, qw2, kw2, consts, "qk_fwd")
    (attn, lse), late_bufs = _attn_fwd(qn, kn, proj, "attn_fwd", rider=_gather_ici_rider(late_bufs, []))
    (u0, uc), late_bufs = _conv_fwd(proj, wts["conv_w"], wts["conv_b"], conv_col0, "conv_fwd",
                                    rider=_gather_forward_rider(late_bufs))
    for n, buf in zip(LATE_GATHER, late_bufs):
        full = buf.reshape(N_CHIPS, -1, buf.shape[3])
        wts[n] = full.reshape(-1, full.shape[2]) if n in ROW_SHARDED else full
    attn_b = attn.astype(BF16)
    y_a = _matmul(attn_b, wts["w_o_attn"], mode="nn", tm=1024, tn=256, tk=512, out_dtype=F32, name="mm_ya",
                  b_blocked=True)
    u3 = _ln_silu_fwd(uc, wts["conv_ln_w"], wts["conv_ln_b"], "ln_fwd")
    y_b = _matmul(u3, wts["w_pw_conv"], mode="nn", tm=1024, tn=256, tk=512, out_dtype=F32, name="mm_yb",
                  b_blocked=True)
    mixed = _gate_fwd(proj, wts["b_gate"], y_a, y_b, gate_col0, "gate_fwd")
    x1 = _matmul(mixed, wts["w_out"], mode="nn", tm=1024, tn=512, tk=1024, out_dtype=F32, name="mm_x1",
                 residual=x)
    h2 = _rmsnorm_fwd(x1, wts["norm2_w"], "rms2_fwd")
    gu = _matmul(h2, wts["w_ffn_in"], mode="nn", tm=512, tn=1408, tk=1024, out_dtype=F32, name="mm_gu",
                 b_blocked=True)
    act = _swiglu_fwd(gu, "swiglu_fwd")
    x2 = _matmul(act, wts["w_ffn_out"], mode="nn", tm=512, tn=512, tk=2816, out_dtype=F32, name="mm_x2",
                 residual=x1)
    dy, dy_b16, loss = _loss_fwd_bwd(x2, target, "loss")

    g = {}
    d_act = _matmul(dy_b16, wts["w_ffn_out"], mode="nt", tm=512, tn=1408, tk=1024, out_dtype=F32, name="mm_dact")
    g["w_ffn_out"] = _matmul(act, dy_b16, mode="tn", tm=1408, tn=512, tk=2048, out_dtype=F32, name="mm_dwffnout")
    dgu = _swiglu_bwd(gu, d_act, "swiglu_bwd")
    dh2 = _matmul(dgu, wts["w_ffn_in"], mode="nt", tm=512, tn=1024, tk=1408, out_dtype=F32, name="mm_dh2",
                  b_blocked=True)
    g["w_ffn_in"] = _matmul(h2, dgu, mode="tn", tm=512, tn=1408, tk=2048, out_dtype=F32, name="mm_dwffnin",
                            out_blocked=N_CHIPS)
    dx1, dx1_b16, g["norm2_w"] = _rmsnorm_bwd(dh2, x1, wts["norm2_w"], dy, "rms2_bwd")
    d_mixed = _matmul(dx1_b16, wts["w_out"], mode="nt", tm=1024, tn=512, tk=1024, out_dtype=F32, name="mm_dmixed")
    g["w_out"] = _matmul(mixed, dx1_b16, mode="tn", tm=512, tn=1024, tk=2048, out_dtype=F32, name="mm_dwout")
    dproj, dy_a, dy_b, db_a, db_b = _gate_bwd(d_mixed, proj, wts["b_gate"], y_a, y_b, gate_col0, "gate_bwd")
    g["b_gate"] = jnp.concatenate([db_a, db_b], axis=0)
    dattn = _matmul(dy_a, wts["w_o_attn"], mode="nt", tm=1024, tn=512, tk=256, out_dtype=F32, name="mm_dattn",
                    b_blocked=True)
    g["w_o_attn"] = _matmul(attn_b, dy_a, mode="tn", tm=512, tn=256, tk=2048, out_dtype=F32, name="mm_dwo",
                            out_blocked=N_CHIPS)
    du3 = _matmul(dy_b, wts["w_pw_conv"], mode="nt", tm=1024, tn=512, tk=256, out_dtype=F32, name="mm_du3",
                  b_blocked=True)
    g["w_pw_conv"] = _matmul(u3, dy_b, mode="tn", tm=512, tn=256, tk=2048, out_dtype=F32, name="mm_dwpw",
                             out_blocked=N_CHIPS)
    duc, g["conv_ln_w"], g["conv_ln_b"] = _ln_silu_bwd(du3, uc, wts["conv_ln_w"], wts["conv_ln_b"], "ln_bwd")

    early = [_blocks_by_half(g.pop(n)) for n in EARLY_REDUCE]
    (da, db, g["conv_w"], g["conv_b"]), from_sibling = _conv_bwd(
        duc, u0, proj, wts["conv_w"], conv_col0, "conv_bwd", rider=_pair_exchange_rider(early, halved=True))
    sums = [_add_own_half(gb, rv, pos_arr, f"grads_pair_add_{n}")
            for n, gb, rv in zip(EARLY_REDUCE, early, from_sibling)]
    (dqn, dkn, dv), by_chip = _attn_bwd(
        qn, kn, proj, dattn, attn, lse, bd, "attn_bwd",
        rider=_chip_exchange_rider([s[0] for s in sums], [s[1] for s in sums]))
    halves = [_sum_chips(bc, pos_arr, f"grads_chip_sum_{n}") for n, bc in zip(EARLY_REDUCE, by_chip)]
    (dproj, dqw, dkw), shards = _qk_bwd(dproj, dqn, dkn, dv, da, db, proj, pos_col, qw2, kw2, consts, "qk_bwd",
                                        rider=_pair_gather_rider(halves))
    reduced = dict(zip(EARLY_REDUCE, shards))
    g["q_norm_w"] = dqw[:, :HEAD_DIM]
    g["k_norm_w"] = dkw[:, :HEAD_DIM]

    c = pos_arr[0]
    rh = h.shape[1] // 2
    h_sibling = lax.dynamic_slice_in_dim(h, (1 - c) * rh, rh, axis=1)
    h_own = lax.dynamic_slice_in_dim(h, c * rh, rh, axis=1)
    g_sibling = _matmul(h_sibling, dproj, mode="tn", tm=rh, tn=640, tk=2048, out_dtype=F32, name="mm_dwin_sibling",
                        out_blocked=N_CHIPS)
    g_own, from_sibling = _matmul(h_own, dproj, mode="tn", tm=rh, tn=640, tk=2048, out_dtype=F32,
                                  name="mm_dwin_own", out_blocked=N_CHIPS,
                                  rider=_pair_exchange_rider([g_sibling], halved=False))
    to_send, own = _add_own_half(g_own, from_sibling[0], pos_arr, "grads_pair_add_w_in")
    dh, by_chip_w_in = _matmul(dproj, wts["w_in"], mode="nt", tm=512, tn=1024, tk=1920, out_dtype=F32, name="mm_dh",
                               b_blocked=True, rider=_chip_exchange_rider([to_send], [own]))
    grad_x, _, g["norm1_w"] = _rmsnorm_bwd(dh, x, wts["norm1_w"], dx1, "rms1_bwd")
    return loss, grad_x, g, reduced, by_chip_w_in[0]


def _mesh_pos():
    return lax.axis_index("x"), lax.axis_index("y"), lax.axis_index("c")


def _other_chips(x, y):
    return [(1 - x, y), (x, 1 - y), (1 - x, 1 - y)]


def _cast_into_slot(shard, chip_arr, dtype, name):
    r, c = shard.shape
    tr = r // 2 if r % 32 == 0 else r

    def body(chip_ref, s_ref, o_ref):
        del chip_ref
        o_ref[...] = s_ref[...].astype(dtype)

    return pl.pallas_call(
        body, name=name,
        grid_spec=pltpu.PrefetchScalarGridSpec(
            num_scalar_prefetch=1, grid=(r // tr,),
            in_specs=[pl.BlockSpec((tr, c), lambda i, chip_ref: (i, 0))],
            out_specs=pl.BlockSpec((None, tr, c), lambda i, chip_ref: (chip_ref[0], i, 0))),
        out_shape=jax.ShapeDtypeStruct((N_CHIPS, r, c), dtype), compiler_params=_params(),
    )(chip_arr, shard)


def _allgather_inplace(big, small, name):
    nb, ns = len(big), len(small)
    n = nb + ns

    def body(*refs):
        bufs = refs[n:2 * n]
        send_sems, recv_sems, fsend_sems, frecv_sems = refs[2 * n:]
        x, y, c = _mesh_pos()
        me = 2 * x + y
        chips = _other_chips(x, y)

        def part(a, slot, half):
            return bufs[a].at[slot, half] if a < nb else bufs[a].at[slot]

        sends = []
        for a in range(n):
            for k, (px, py) in enumerate(chips):
                cp = pltpu.make_async_remote_copy(
                    src_ref=part(a, me, c), dst_ref=part(a, me, c), send_sem=send_sems.at[a, k],
                    recv_sem=recv_sems.at[a, k], device_id=(px, py, c), device_id_type=MESH)
                cp.start()
                sends.append(cp)
        for a in range(n):
            for k, (px, py) in enumerate(chips):
                slot = 2 * px + py
                pltpu.make_async_remote_copy(
                    src_ref=part(a, slot, c), dst_ref=part(a, slot, c), send_sem=send_sems.at[a, k],
                    recv_sem=recv_sems.at[a, k], device_id=(px, py, c), device_id_type=MESH).wait_recv()
                if a < nb:
                    fwd = pltpu.make_async_remote_copy(
                        src_ref=part(a, slot, c), dst_ref=part(a, slot, c), send_sem=fsend_sems.at[a, k],
                        recv_sem=frecv_sems.at[a, k], device_id=(x, y, 1 - c), device_id_type=MESH)
                    fwd.start()
                    sends.append(fwd)
        for a in range(nb):
            for k, (px, py) in enumerate(chips):
                slot = 2 * px + py
                pltpu.make_async_remote_copy(
                    src_ref=part(a, slot, 1 - c), dst_ref=part(a, slot, 1 - c), send_sem=fsend_sems.at[a, k],
                    recv_sem=frecv_sems.at[a, k], device_id=(x, y, 1 - c), device_id_type=MESH).wait_recv()
        for cp in sends:
            cp.wait_send()

    ops = list(big) + list(small)
    return pl.pallas_call(
        body, name=name, in_specs=[ANY] * n, out_specs=[ANY] * n,
        out_shape=[jax.ShapeDtypeStruct(o.shape, o.dtype) for o in ops],
        input_output_aliases={i: i for i in range(n)},
        scratch_shapes=[pltpu.SemaphoreType.DMA((n, 3)), pltpu.SemaphoreType.DMA((n, 3)),
                        pltpu.SemaphoreType.DMA((nb, 3)), pltpu.SemaphoreType.DMA((nb, 3))],
    )(*ops)


def _comm_call(rider, name):
    def body():
        pass

    return _pallas(body, name=name, grid=(1,), in_specs=[], out_specs=[], out_shape=[], operands=[],
                   rider=rider)[1]


def _gather_ici_rider(big, small):
    nb = len(big)
    n = nb + len(small)

    def copies(bufs, sems):
        x, y, c = _mesh_pos()
        me = 2 * x + y
        part = lambda a, slot: bufs[a].at[slot, c] if a < nb else bufs[a].at[slot]
        out = []
        for a in range(n):
            for k, (px, py) in enumerate(_other_chips(x, y)):
                send = functools.partial(
                    pltpu.make_async_remote_copy,
                    src_ref=part(a, me), dst_ref=part(a, me), send_sem=sems[0].at[a, k],
                    recv_sem=sems[1].at[a, k], device_id=(px, py, c), device_id_type=MESH)
                recv = functools.partial(
                    pltpu.make_async_remote_copy,
                    src_ref=part(a, 2 * px + py), dst_ref=part(a, 2 * px + py), send_sem=sems[0].at[a, k],
                    recv_sem=sems[1].at[a, k], device_id=(px, py, c), device_id_type=MESH)
                out.append((send, recv))
        return out

    def start(r_in, r_out, sems):
        for send, _ in copies(r_out, sems):
            send().start()

    def wait(r_in, r_out, sems):
        cps = copies(r_out, sems)
        for _, recv in cps:
            recv().wait_recv()
        for send, _ in cps:
            send().wait_send()

    ops = list(big) + list(small)
    return _Rider(ops, [jax.ShapeDtypeStruct(o.shape, o.dtype) for o in ops], {i: i for i in range(n)},
                  [pltpu.SemaphoreType.DMA((n, 3)), pltpu.SemaphoreType.DMA((n, 3))], start, wait)


def _gather_forward_rider(big):
    n = len(big)

    def copies(bufs, sems):
        x, y, c = _mesh_pos()
        out = []
        for a in range(n):
            for k, (px, py) in enumerate(_other_chips(x, y)):
                slot = 2 * px + py
                send = functools.partial(
                    pltpu.make_async_remote_copy,
                    src_ref=bufs[a].at[slot, c], dst_ref=bufs[a].at[slot, c], send_sem=sems[0].at[a, k],
                    recv_sem=sems[1].at[a, k], device_id=(x, y, 1 - c), device_id_type=MESH)
                recv = functools.partial(
                    pltpu.make_async_remote_copy,
                    src_ref=bufs[a].at[slot, 1 - c], dst_ref=bufs[a].at[slot, 1 - c], send_sem=sems[0].at[a, k],
                    recv_sem=sems[1].at[a, k], device_id=(x, y, 1 - c), device_id_type=MESH)
                out.append((send, recv))
        return out

    def start(r_in, r_out, sems):
        for send, _ in copies(r_out, sems):
            send().start()

    def wait(r_in, r_out, sems):
        cps = copies(r_out, sems)
        for _, recv in cps:
            recv().wait_recv()
        for send, _ in cps:
            send().wait_send()

    return _Rider(big, [jax.ShapeDtypeStruct(o.shape, o.dtype) for o in big], {i: i for i in range(n)},
                  [pltpu.SemaphoreType.DMA((n, 3)), pltpu.SemaphoreType.DMA((n, 3))], start, wait)


def _pair_exchange_rider(gs, halved):
    n = len(gs)

    def copies(r_in, r_out, sems):
        x, y, c = _mesh_pos()
        return [pltpu.make_async_remote_copy(
            src_ref=r_in[a].at[:, 1 - c] if halved else r_in[a], dst_ref=r_out[a], send_sem=sems[0].at[a],
            recv_sem=sems[1].at[a], device_id=(x, y, 1 - c), device_id_type=MESH) for a in range(n)]

    def start(r_in, r_out, sems):
        for cp in copies(r_in, r_out, sems):
            cp.start()

    def wait(r_in, r_out, sems):
        for cp in copies(r_in, r_out, sems):
            cp.wait()

    return _Rider(gs, [jax.ShapeDtypeStruct((g.shape[0],) + g.shape[-2:], g.dtype) for g in gs], {},
                  [pltpu.SemaphoreType.DMA((n,)), pltpu.SemaphoreType.DMA((n,))], start, wait)


def _chip_exchange_rider(to_send, by_chip):
    n = len(to_send)

    def copies(r_in, r_out, sems):
        x, y, c = _mesh_pos()
        me = 2 * x + y
        out = []
        for a in range(n):
            for k, (px, py) in enumerate(_other_chips(x, y)):
                send = functools.partial(
                    pltpu.make_async_remote_copy,
                    src_ref=r_in[a].at[2 * px + py], dst_ref=r_out[a].at[me], send_sem=sems[0].at[a, k],
                    recv_sem=sems[1].at[a, k], device_id=(px, py, c), device_id_type=MESH)
                recv = functools.partial(
                    pltpu.make_async_remote_copy,
                    src_ref=r_in[a].at[me], dst_ref=r_out[a].at[2 * px + py], send_sem=sems[0].at[a, k],
                    recv_sem=sems[1].at[a, k], device_id=(px, py, c), device_id_type=MESH)
                out.append((send, recv))
        return out

    def start(r_in, r_out, sems):
        for send, _ in copies(r_in, r_out, sems):
            send().start()

    def wait(r_in, r_out, sems):
        cps = copies(r_in, r_out, sems)
        for _, recv in cps:
            recv().wait_recv()
        for send, _ in cps:
            send().wait_send()

    return _Rider(list(to_send) + list(by_chip), [jax.ShapeDtypeStruct(b.shape, b.dtype) for b in by_chip],
                  {n + i: i for i in range(n)},
                  [pltpu.SemaphoreType.DMA((n, 3)), pltpu.SemaphoreType.DMA((n, 3))], start, wait)


def _pair_gather_rider(bufs):
    n = len(bufs)

    def copies(r_out, sems):
        x, y, c = _mesh_pos()
        out = []
        for a in range(n):
            send = functools.partial(
                    pltpu.make_async_remote_copy,
                src_ref=r_out[a].at[c], dst_ref=r_out[a].at[c], send_sem=sems[0].at[a],
                recv_sem=sems[1].at[a], device_id=(x, y, 1 - c), device_id_type=MESH)
            recv = functools.partial(
                    pltpu.make_async_remote_copy,
                src_ref=r_out[a].at[1 - c], dst_ref=r_out[a].at[1 - c], send_sem=sems[0].at[a],
                recv_sem=sems[1].at[a], device_id=(x, y, 1 - c), device_id_type=MESH)
            out.append((send, recv))
        return out

    def start(r_in, r_out, sems):
        for send, _ in copies(r_out, sems):
            send().start()

    def wait(r_in, r_out, sems):
        cps = copies(r_out, sems)
        for _, recv in cps:
            recv().wait_recv()
        for send, _ in cps:
            send().wait_send()

    return _Rider(bufs, [jax.ShapeDtypeStruct(b.shape, b.dtype) for b in bufs], {i: i for i in range(n)},
                  [pltpu.SemaphoreType.DMA((n,)), pltpu.SemaphoreType.DMA((n,))], start, wait)


def _add_own_half(g, recv, pos_arr, name):
    nb, rh, cols = g.shape[0], g.shape[-2], g.shape[-1]

    def body(pos_ref, g_ref, r_ref, send_ref, own_ref):
        s = (g_ref[...] + r_ref[...]).astype(BF16)
        send_ref[...] = s

        @pl.when(pl.program_id(0) == pos_ref[1])
        def _():
            own_ref[...] = s

    blk = pl.BlockSpec((None, rh, cols), lambda j, pos_ref: (j, 0, 0))
    g_spec = blk if g.ndim == 3 else pl.BlockSpec((None, None, rh, cols),
                                                   lambda j, pos_ref: (j, pos_ref[0], 0, 0))
    shape = jax.ShapeDtypeStruct((nb, rh, cols), BF16)
    return pl.pallas_call(
        body, name=name,
        grid_spec=pltpu.PrefetchScalarGridSpec(
            num_scalar_prefetch=1, grid=(nb,), in_specs=[g_spec, blk],
            out_specs=[blk, pl.BlockSpec((None, rh, cols), lambda j, pos_ref: (pos_ref[1], 0, 0))]),
        out_shape=[shape, shape], compiler_params=_params(),
    )(pos_arr, g, recv)


def _sum_chips(gath, pos_arr, name):
    nb, rh, cols = gath.shape

    def body(pos_ref, a_ref, b_ref, c_ref, d_ref, o_ref):
        del pos_ref
        o_ref[...] = ((a_ref[...].astype(F32) + b_ref[...].astype(F32)) + c_ref[...].astype(F32)) \
            + d_ref[...].astype(F32)

    tr = rh // 2 if (rh // 2) % 16 == 0 else rh
    specs = [pl.BlockSpec((None, tr, cols), functools.partial(lambda i, pos_ref, j: (j, i, 0), j=j))
             for j in range(nb)]
    return pl.pallas_call(
        body, name=name,
        grid_spec=pltpu.PrefetchScalarGridSpec(
            num_scalar_prefetch=1, grid=(rh // tr,), in_specs=specs,
            out_specs=pl.BlockSpec((None, tr, cols), lambda i, pos_ref: (pos_ref[0], i, 0))),
        out_shape=jax.ShapeDtypeStruct((2, rh, cols), F32), compiler_params=_params(),
    )(pos_arr, gath, gath, gath, gath)


def _small_allreduce(v, name):
    n = v.shape[0]
    n_dev = 8

    def body(v_ref, o_ref, buf, send_sems, recv_sems):
        x, y, c = _mesh_pos()
        me = 4 * x + 2 * y + c
        buf[me] = v_ref[...]
        sends = []
        peers = []
        for r in range(1, n_dev):
            px = 1 - x if r & 4 else x
            py = 1 - y if r & 2 else y
            pc = 1 - c if r & 1 else c
            peers.append((px, py, pc))
            cp = pltpu.make_async_remote_copy(
                src_ref=v_ref, dst_ref=buf.at[me], send_sem=send_sems.at[r - 1],
                recv_sem=recv_sems.at[r - 1], device_id=(px, py, pc), device_id_type=MESH)
            cp.start()
            sends.append(cp)
        for r, (px, py, pc) in enumerate(peers):
            pltpu.make_async_remote_copy(
                src_ref=v_ref, dst_ref=buf.at[4 * px + 2 * py + pc], send_sem=send_sems.at[r],
                recv_sem=recv_sems.at[r], device_id=(px, py, pc), device_id_type=MESH).wait_recv()
        for cp in sends:
            cp.wait_send()
        acc = buf[0]
        for i in range(1, n_dev):
            acc = acc + buf[i]
        o_ref[...] = acc

    vm = pl.BlockSpec(memory_space=pltpu.VMEM)
    return pl.pallas_call(
        body, name=name, in_specs=[vm], out_specs=vm, out_shape=jax.ShapeDtypeStruct(v.shape, v.dtype),
        scratch_shapes=[pltpu.VMEM((n_dev, n, LANES), F32), pltpu.SemaphoreType.DMA((n_dev - 1,)),
                        pltpu.SemaphoreType.DMA((n_dev - 1,))],
        compiler_params=_params(),
    )(v)


def _adamw_math(w, g, m, v):
    m = ADAM_B1 * m + (1.0 - ADAM_B1) * g
    v = ADAM_B2 * v + (1.0 - ADAM_B2) * (g * g)
    m_hat = m / (1.0 - ADAM_B1 ** ADAM_STEP)
    v_hat = v / (1.0 - ADAM_B2 ** ADAM_STEP)
    delta = -ADAM_LR * (m_hat / (jnp.sqrt(v_hat) + ADAM_EPS) + ADAM_WD * w)
    return delta, m, v


def _adamw(w, g, m, v, name, rider=None):
    r, c = w.shape
    tr = 128 if r % 128 == 0 else 64
    assert r % tr == 0

    def body(w_ref, g_ref, m_ref, v_ref, d_ref, mo_ref, vo_ref):
        d, mn, vn = _adamw_math(w_ref[...], g_ref[...], m_ref[...], v_ref[...])
        d_ref[...] = d
        mo_ref[...] = mn
        vo_ref[...] = vn

    blk = pl.BlockSpec((tr, c), lambda i: (i, 0))
    return _pallas(body, name=name, grid=(r // tr,), in_specs=[blk] * 4, out_specs=[blk] * 3,
                   out_shape=[jax.ShapeDtypeStruct((r, c), F32)] * 3, operands=[w, g, m, v], rider=rider)


def _adamw_small(ws, gs, ms, vs, name):
    n = len(ws)

    def body(*refs):
        w_r, g_r, m_r, v_r = refs[:n], refs[n:2 * n], refs[2 * n:3 * n], refs[3 * n:4 * n]
        d_o, m_o, v_o = refs[4 * n:5 * n], refs[5 * n:6 * n], refs[6 * n:7 * n]
        for i in range(n):
            d, mn, vn = _adamw_math(w_r[i][...], g_r[i][...], m_r[i][...], v_r[i][...])
            d_o[i][...] = d
            m_o[i][...] = mn
            v_o[i][...] = vn

    vm = pl.BlockSpec(memory_space=pltpu.VMEM)
    shapes = [jax.ShapeDtypeStruct(w.shape, F32) for w in ws]
    outs = pl.pallas_call(
        body, name=name, in_specs=[vm] * (4 * n), out_specs=[vm] * (3 * n), out_shape=shapes * 3,
        compiler_params=_params(),
    )(*ws, *gs, *ms, *vs)
    return outs[:n], outs[n:2 * n], outs[2 * n:]


BIG = ("w_in", "w_o_attn", "w_pw_conv", "w_out", "w_ffn_in", "w_ffn_out")
ROW_SHARDED = ("w_out", "w_ffn_out")
SMALL = ("norm1_w", "b_gate", "q_norm_w", "k_norm_w", "conv_w", "conv_b", "conv_ln_w", "conv_ln_b", "norm2_w")
ORDER = ("norm1_w", "w_in", "b_gate", "q_norm_w", "k_norm_w", "w_o_attn", "conv_w", "conv_b", "conv_ln_w",
         "conv_ln_b", "w_pw_conv", "w_out", "norm2_w", "w_ffn_in", "w_ffn_out")
PACK_TILE = 8 * LANES


def _pack_small(parts):
    rows = []
    for p in parts:
        flat = p.reshape(-1)
        pad = (-flat.shape[0]) % PACK_TILE
        rows.append(jnp.pad(flat, (0, pad)).reshape(-1, LANES))
    return jnp.concatenate(rows, axis=0)


def _unpack_small(packed, shapes):
    out, row = [], 0
    for shp in shapes:
        size = int(np.prod(shp))
        nrow = -(-size // PACK_TILE) * (PACK_TILE // LANES)
        out.append(packed[row:row + nrow].reshape(-1)[:size].reshape(shp))
        row += nrow
    return out


def kernel(x, positions, norm1_w, w_in, b_gate, q_norm_w, k_norm_w, w_o_attn, conv_w, conv_b, conv_ln_w, conv_ln_b, w_pw_conv, w_out, norm2_w, w_ffn_in, w_ffn_out, loss_target, m_norm1_w, m_w_in, m_b_gate, m_q_norm_w, m_k_norm_w, m_w_o_attn, m_conv_w, m_conv_b, m_conv_ln_w, m_conv_ln_b, m_w_pw_conv, m_w_out, m_norm2_w, m_w_ffn_in, m_w_ffn_out, v_norm1_w, v_w_in, v_b_gate, v_q_norm_w, v_k_norm_w, v_w_o_attn, v_conv_w, v_conv_b, v_conv_ln_w, v_conv_ln_b, v_w_pw_conv, v_w_out, v_norm2_w, v_w_ffn_in, v_w_ffn_out):
    w = dict(norm1_w=norm1_w, w_in=w_in, b_gate=b_gate, q_norm_w=q_norm_w, k_norm_w=k_norm_w, w_o_attn=w_o_attn,
             conv_w=conv_w, conv_b=conv_b, conv_ln_w=conv_ln_w, conv_ln_b=conv_ln_b, w_pw_conv=w_pw_conv,
             w_out=w_out, norm2_w=norm2_w, w_ffn_in=w_ffn_in, w_ffn_out=w_ffn_out)
    m = dict(norm1_w=m_norm1_w, w_in=m_w_in, b_gate=m_b_gate, q_norm_w=m_q_norm_w, k_norm_w=m_k_norm_w,
             w_o_attn=m_w_o_attn, conv_w=m_conv_w, conv_b=m_conv_b, conv_ln_w=m_conv_ln_w,
             conv_ln_b=m_conv_ln_b, w_pw_conv=m_w_pw_conv, w_out=m_w_out, norm2_w=m_norm2_w,
             w_ffn_in=m_w_ffn_in, w_ffn_out=m_w_ffn_out)
    v = dict(norm1_w=v_norm1_w, w_in=v_w_in, b_gate=v_b_gate, q_norm_w=v_q_norm_w, k_norm_w=v_k_norm_w,
             w_o_attn=v_w_o_attn, conv_w=v_conv_w, conv_b=v_conv_b, conv_ln_w=v_conv_ln_w,
             conv_ln_b=v_conv_ln_b, w_pw_conv=v_w_pw_conv, w_out=v_w_out, norm2_w=v_norm2_w,
             w_ffn_in=v_w_ffn_in, w_ffn_out=v_w_ffn_out)
    cx, cy, cc = _mesh_pos()
    chip = 2 * cx + cy

    chip_arr = chip.reshape(1).astype(jnp.int32)
    pos_arr = jnp.stack([cc, chip]).astype(jnp.int32)
    bufs = {}
    for n in BIG:
        buf = _cast_into_slot(w[n][0], chip_arr, BF16, f"cast_{n}")
        bufs[n] = buf.reshape(N_CHIPS, 2, buf.shape[1] // 2, buf.shape[2])
    small_bufs = [_cast_into_slot(w[n][0], chip_arr, F32, f"slot_{n}") for n in ("conv_w", "b_gate")]
    w_in_buf, conv_w_buf, b_gate_buf = _allgather_inplace([bufs["w_in"]], small_bufs, "allgather_w_in")
    wts = dict(w_in=w_in_buf.reshape(N_CHIPS, -1, w_in_buf.shape[3]),
               conv_w=conv_w_buf.transpose(1, 0, 2).reshape(CONV_WIDTH, -1),
               b_gate=b_gate_buf.transpose(1, 0, 2).reshape(2, 1, -1),
               norm1_w=norm1_w, q_norm_w=q_norm_w, k_norm_w=k_norm_w, conv_b=conv_b, conv_ln_w=conv_ln_w,
               conv_ln_b=conv_ln_b, norm2_w=norm2_w)

    loss, grad_x, g, reduced, by_chip_w_in = _forward_backward(
        x[0], positions.reshape(-1, 1), loss_target[0], wts, [bufs[n] for n in LATE_GATHER], pos_arr)
    grads = {n: b.reshape(-1, b.shape[2]) for n, b in reduced.items()}
    half_w_in = _sum_chips(by_chip_w_in, pos_arr, "grads_chip_sum_w_in")

    small_parts = [loss] + [g[n] for n in SMALL]
    small_shapes = [p.shape for p in small_parts]
    reduced = _unpack_small(_small_allreduce(_pack_small(small_parts), "small_allreduce"), small_shapes)
    loss_total = reduced[0].reshape(())
    for n, r in zip(SMALL, reduced[1:]):
        grads[n] = r
    ch_shard = conv_w.shape[2]
    grads["conv_w"] = lax.dynamic_slice_in_dim(grads["conv_w"], chip * ch_shard, ch_shard, axis=1)
    d_shard = b_gate.shape[2]
    grads["b_gate"] = lax.dynamic_slice_in_dim(grads["b_gate"], chip * d_shard, d_shard, axis=1)

    delta, new_m, new_v = {}, {}, {}
    for n in EARLY_REDUCE:
        rider = _pair_gather_rider([half_w_in]) if n == "w_ffn_in" else None
        res = _adamw(w[n][0], grads[n], m[n][0], v[n][0], f"adamw_{n}", rider=rider)
        if rider is not None:
            res, (shard_w_in,) = res
            grads["w_in"] = shard_w_in.reshape(-1, shard_w_in.shape[2])
        delta[n], new_m[n], new_v[n] = res
    delta["w_in"], new_m["w_in"], new_v["w_in"] = _adamw(w["w_in"][0], grads["w_in"], m["w_in"][0], v["w_in"][0],
                                                          "adamw_w_in")
    flat2 = lambda a: a.reshape(-1, a.shape[-1])
    d_s, m_s, v_s = _adamw_small([flat2(w[n]) for n in SMALL], [flat2(grads[n]) for n in SMALL],
                                 [flat2(m[n]) for n in SMALL], [flat2(v[n]) for n in SMALL], "adamw_small")
    for i, n in enumerate(SMALL):
        delta[n], new_m[n], new_v[n] = d_s[i], m_s[i], v_s[i]

    shaped = lambda d, n: d[n].reshape(w[n].shape)
    return (loss_total, grad_x[None], *[shaped(grads, n) for n in ORDER], *[shaped(delta, n) for n in ORDER],
            *[shaped(new_m, n) for n in ORDER], *[shaped(new_v, n) for n in ORDER])
```

```python
import functools

import numpy as np
import jax
import jax.numpy as jnp
from jax import lax
from jax.experimental import pallas as pl
from jax.experimental.pallas import tpu as pltpu

F32 = jnp.float32
BF16 = jnp.bfloat16
MESH = pl.DeviceIdType.MESH
ANY = pl.BlockSpec(memory_space=pl.ANY)

HEAD_DIM = 64
N_SLOT_HEADS = 8
DILATIONS = (1, 4, 16)
HALF_SPAN = 64
ROPE_THETA = 500000.0
ROT_DIM = 16
CONV_WIDTH = 31
EPS = 1e-6
NEG_INF = -1e30
ADAM_LR, ADAM_B1, ADAM_B2, ADAM_EPS, ADAM_WD, ADAM_STEP = 0.001, 0.9, 0.999, 1e-08, 0.01, 10

LANES = 128
QBLK = 128
KWIN = QBLK + 2 * HALF_SPAN
VMEM_LIMIT = 48 * 1024 * 1024
N_CHIPS = 4
HIGHEST = lax.Precision.HIGHEST


def _params(**kw):
    return pltpu.CompilerParams(vmem_limit_bytes=VMEM_LIMIT, **kw)


class _Rider:
    def __init__(self, operands, out_shapes, aliases, scratch, start, wait):
        self.operands, self.out_shapes, self.aliases = list(operands), list(out_shapes), dict(aliases)
        self.scratch, self.start, self.wait = list(scratch), start, wait


def _pallas(body, *, name, grid, in_specs, out_specs, out_shape, operands, scratch_shapes=(), aliases=None,
            rider=None):
    single = not isinstance(out_specs, (list, tuple))
    out_specs_l = [out_specs] if single else list(out_specs)
    out_shape_l = [out_shape] if single else list(out_shape)
    aliases = dict(aliases or {})
    if rider is None:
        res = pl.pallas_call(
            body, name=name, grid=grid, in_specs=list(in_specs), out_specs=out_specs_l, out_shape=out_shape_l,
            scratch_shapes=list(scratch_shapes), input_output_aliases=aliases, compiler_params=_params(),
        )(*operands)
        return res[0] if single else res
    n_in, n_rin = len(in_specs), len(rider.operands)
    n_out, n_rout = len(out_specs_l), len(rider.out_shapes)
    n_sc = len(scratch_shapes)

    def wrapped(*refs):
        main_in, r_in = refs[:n_in], refs[n_in:n_in + n_rin]
        o0 = n_in + n_rin
        main_out, r_out = refs[o0:o0 + n_out], refs[o0 + n_out:o0 + n_out + n_rout]
        s0 = o0 + n_out + n_rout
        main_sc, r_sc = refs[s0:s0 + n_sc], refs[s0 + n_sc:]
        ids = [pl.program_id(d) for d in range(len(grid))]
        first = functools.reduce(jnp.logical_and, [i == 0 for i in ids])
        last = functools.reduce(jnp.logical_and, [i == n - 1 for i, n in zip(ids, grid)])

        @pl.when(first)
        def _():
            rider.start(r_in, r_out, r_sc)

        body(*main_in, *main_out, *main_sc)

        @pl.when(last)
        def _():
            rider.wait(r_in, r_out, r_sc)

    for src, dst in rider.aliases.items():
        aliases[n_in + src] = n_out + dst
    res = pl.pallas_call(
        wrapped, name=name, grid=grid, in_specs=list(in_specs) + [ANY] * n_rin,
        out_specs=out_specs_l + [ANY] * n_rout, out_shape=out_shape_l + rider.out_shapes,
        scratch_shapes=list(scratch_shapes) + rider.scratch, input_output_aliases=aliases,
        compiler_params=_params(),
    )(*operands, *rider.operands)
    main = res[:n_out]
    return (main[0] if single else main), res[n_out:]


def _matmul(a, b, *, mode, tm, tn, tk, out_dtype, name, b_blocked=False,
            out_blocked=None, residual=None, rider=None):
    a_shape = a.shape
    if mode == "nn":
        m_dim, k_dim = a_shape
        n_dim = b.shape[0] * b.shape[2] if b_blocked else b.shape[1]
        rows, cols, red = m_dim, n_dim, k_dim
    elif mode == "nt":
        m_dim, n_dim = a_shape
        k_dim = b.shape[1] if b_blocked else b.shape[0]
        rows, cols, red = m_dim, k_dim, n_dim
    else:
        m_dim, k_dim = a_shape
        n_dim = b.shape[1]
        rows, cols, red = k_dim, n_dim, m_dim
    assert rows % tm == 0 and cols % tn == 0 and red % tk == 0, (name, rows, cols, red)
    ni, nj, nk = rows // tm, cols // tn, red // tk

    if mode == "nn":
        a_spec = pl.BlockSpec((tm, tk), lambda i, j, k: (i, k))
        if b_blocked:
            per = b.shape[2] // tn
            b_spec = pl.BlockSpec((None, tk, tn), lambda i, j, k: (j // per, k, j % per))
        else:
            b_spec = pl.BlockSpec((tk, tn), lambda i, j, k: (k, j))
        dims = (((1,), (0,)), ((), ()))
    elif mode == "nt":
        a_spec = pl.BlockSpec((tm, tk), lambda i, j, k: (i, k))
        if b_blocked:
            per = b.shape[2] // tk
            b_spec = pl.BlockSpec((None, tn, tk), lambda i, j, k: (k // per, j, k % per))
        else:
            b_spec = pl.BlockSpec((tn, tk), lambda i, j, k: (j, k))
        dims = (((1,), (1,)), ((), ()))
    else:
        a_spec = pl.BlockSpec((tk, tm), lambda i, j, k: (k, i))
        b_spec = pl.BlockSpec((tk, tn), lambda i, j, k: (k, j))
        dims = (((0,), (0,)), ((), ()))

    if out_blocked:
        per_o = (cols // out_blocked) // tn
        out_spec = pl.BlockSpec((None, tm, tn), lambda i, j, k: (j // per_o, i, j % per_o))
        out_shape = jax.ShapeDtypeStruct((out_blocked, rows, cols // out_blocked), out_dtype)
    else:
        out_spec = pl.BlockSpec((tm, tn), lambda i, j, k: (i, j))
        out_shape = jax.ShapeDtypeStruct((rows, cols), out_dtype)

    in_specs = [a_spec, b_spec]
    operands = [a, b]
    if residual is not None:
        in_specs.append(pl.BlockSpec((tm, tn), lambda i, j, k: (i, j)))
        operands.append(residual)
    has_res = residual is not None

    def body(*refs):
        a_ref, b_ref = refs[0], refs[1]
        res_ref = refs[2] if has_res else None
        o_ref = refs[3] if has_res else refs[2]
        prod = lax.dot_general(a_ref[...], b_ref[...], dims, preferred_element_type=F32)

        def finish(val):
            if has_res:
                val = val + res_ref[...]
            o_ref[...] = val.astype(out_dtype)

        if nk == 1:
            finish(prod)
        else:
            acc_ref = refs[-1]
            k = pl.program_id(2)

            @pl.when(k == 0)
            def _():
                acc_ref[...] = prod

            @pl.when(k > 0)
            def _():
                acc_ref[...] += prod

            @pl.when(k == nk - 1)
            def _():
                finish(acc_ref[...])

    scratch = [pltpu.VMEM((tm, tn), F32)] if nk > 1 else []
    return _pallas(body, name=name, grid=(ni, nj, nk), in_specs=in_specs, out_specs=out_spec,
                   out_shape=out_shape, operands=operands, scratch_shapes=scratch, rider=rider)


def _rmsnorm_fwd(x, w, name):
    s, d = x.shape
    tm = 256

    def body(x_ref, w_ref, o_ref):
        xv = x_ref[...]
        rstd = lax.rsqrt(jnp.mean(xv * xv, axis=-1, keepdims=True) + EPS)
        o_ref[...] = (xv * rstd * w_ref[...]).astype(BF16)

    return pl.pallas_call(
        body, name=name, grid=(s // tm,),
        in_specs=[pl.BlockSpec((tm, d), lambda i: (i, 0)), pl.BlockSpec((1, d), lambda i: (0, 0))],
        out_specs=pl.BlockSpec((tm, d), lambda i: (i, 0)),
        out_shape=jax.ShapeDtypeStruct((s, d), BF16), compiler_params=_params(),
    )(x, w)


def _rmsnorm_bwd(dh, x, w, dres, name):
    s, d = x.shape
    tm = 256

    def body(dh_ref, x_ref, w_ref, dres_ref, dx_ref, dxb_ref, dw_ref):
        xv = x_ref[...]
        rstd = lax.rsqrt(jnp.mean(xv * xv, axis=-1, keepdims=True) + EPS)
        xhat = xv * rstd
        dhv = dh_ref[...]
        g = dhv * w_ref[...]
        dx = rstd * (g - xhat * jnp.mean(g * xhat, axis=-1, keepdims=True)) + dres_ref[...]
        dx_ref[...] = dx
        dxb_ref[...] = dx.astype(BF16)
        part = jnp.sum(dhv * xhat, axis=0, keepdims=True)

        @pl.when(pl.program_id(0) == 0)
        def _():
            dw_ref[...] = part

        @pl.when(pl.program_id(0) > 0)
        def _():
            dw_ref[...] += part

    row = pl.BlockSpec((tm, d), lambda i: (i, 0))
    vec = pl.BlockSpec((1, d), lambda i: (0, 0))
    return pl.pallas_call(
        body, name=name, grid=(s // tm,), in_specs=[row, row, vec, row], out_specs=[row, row, vec],
        out_shape=[jax.ShapeDtypeStruct((s, d), F32), jax.ShapeDtypeStruct((s, d), BF16),
                   jax.ShapeDtypeStruct((1, d), F32)],
        compiler_params=_params(),
    )(dh, x, w, dres)


def _rope_consts():
    lane = np.arange(LANES)
    in_head = lane % HEAD_DIM
    inv_freq = ROPE_THETA ** (-jnp.arange(0, ROT_DIM, 2, dtype=F32) / ROT_DIM)
    invf = jnp.where(jnp.asarray(in_head < ROT_DIM), jnp.tile(inv_freq, LANES // (ROT_DIM // 2)), 0.0)
    m_a = np.where(in_head < ROT_DIM // 2, -1.0, 0.0).astype(np.float32)
    m_b = np.where((in_head >= ROT_DIM // 2) & (in_head < ROT_DIM), 1.0, 0.0).astype(np.float32)
    block_diag = (lane[:, None] // HEAD_DIM == lane[None, :] // HEAD_DIM).astype(np.float32)
    return (invf.reshape(1, LANES).astype(F32), jnp.asarray(m_a).reshape(1, LANES),
            jnp.asarray(m_b).reshape(1, LANES), jnp.asarray(block_diag))


def _head_sums(v, bd):
    return jnp.dot(v, bd, precision=HIGHEST, preferred_element_type=F32)


def _qk_fwd(proj, pos_col, qw2, kw2, consts, name, rider=None):
    s = proj.shape[0]
    width = 3 * N_SLOT_HEADS * HEAD_DIM
    tm = 256
    invf, m_a, m_b, bd = consts
    scale = HEAD_DIM ** -0.5

    def body(q_ref, k_ref, pos_ref, qw_ref, kw_ref, invf_ref, ma_ref, mb_ref, bd_ref, qo_ref, ko_ref):
        ang = pos_ref[...].astype(F32) * invf_ref[...]
        cos = jnp.cos(ang)
        sin = jnp.sin(ang)
        s_a = sin * ma_ref[...]
        s_b = sin * mb_ref[...]
        bdv = bd_ref[...]
        for src, w_ref, dst, sc in ((q_ref, qw_ref, qo_ref, scale), (k_ref, kw_ref, ko_ref, 1.0)):
            for cb in range(width // LANES):
                cols = slice(cb * LANES, (cb + 1) * LANES)
                t = src[:, cols]
                rstd = lax.rsqrt(_head_sums(t * t, bdv) * (1.0 / HEAD_DIM) + EPS)
                y = t * rstd * w_ref[...]
                r = y * cos + pltpu.roll(y, LANES - 8, axis=1) * s_a + pltpu.roll(y, 8, axis=1) * s_b
                dst[:, cols] = r * sc if sc != 1.0 else r

    vec = pl.BlockSpec((1, LANES), lambda i: (0, 0))
    return _pallas(
        body, name=name, grid=(s // tm,),
        in_specs=[pl.BlockSpec((tm, width), lambda i: (i, 0)), pl.BlockSpec((tm, width), lambda i: (i, 1)),
                  pl.BlockSpec((tm, 1), lambda i: (i, 0)), vec, vec, vec, vec, vec,
                  pl.BlockSpec((LANES, LANES), lambda i: (0, 0))],
        out_specs=[pl.BlockSpec((tm, width), lambda i: (i, 0))] * 2,
        out_shape=[jax.ShapeDtypeStruct((s, width), F32)] * 2,
        operands=[proj, proj, pos_col, qw2, kw2, invf, m_a, m_b, bd], rider=rider)


def _qk_bwd(dproj, dqn, dkn, dv, da, db, proj, pos_col, qw2, kw2, consts, name, rider=None):
    s = proj.shape[0]
    width = 3 * N_SLOT_HEADS * HEAD_DIM
    ch = da.shape[1]
    out_w = 3 * width + 2 * ch
    tm = 256
    invf, m_a, m_b, bd = consts
    scale = HEAD_DIM ** -0.5

    def body(dproj_in, dq_ref, dk_ref, dv_ref, da_ref, db_ref, q_ref, k_ref, pos_ref, qw_ref, kw_ref,
             invf_ref, ma_ref, mb_ref, bd_ref, out_ref, dqw_ref, dkw_ref):
        del dproj_in
        ang = pos_ref[...].astype(F32) * invf_ref[...]
        cos = jnp.cos(ang)
        sin = jnp.sin(ang)
        s_a = sin * ma_ref[...]
        s_b = sin * mb_ref[...]
        bdv = bd_ref[...]
        first = pl.program_id(0) == 0
        for src, dsrc, w_ref, col0, dw_ref, sc in ((q_ref, dq_ref, qw_ref, 0, dqw_ref, scale),
                                                   (k_ref, dk_ref, kw_ref, width, dkw_ref, 1.0)):
            dw_acc = jnp.zeros((1, LANES), F32)
            for cb in range(width // LANES):
                cols = slice(cb * LANES, (cb + 1) * LANES)
                t = src[:, cols]
                dr = dsrc[:, cols]
                if sc != 1.0:
                    dr = dr * sc
                dy = dr * cos + pltpu.roll(dr * s_a, 8, axis=1) + pltpu.roll(dr * s_b, LANES - 8, axis=1)
                rstd = lax.rsqrt(_head_sums(t * t, bdv) * (1.0 / HEAD_DIM) + EPS)
                xhat = t * rstd
                g = dy * w_ref[...]
                dt = rstd * (g - xhat * (_head_sums(g * xhat, bdv) * (1.0 / HEAD_DIM)))
                out_ref[:, col0 + cb * LANES: col0 + (cb + 1) * LANES] = dt.astype(BF16)
                dw_acc = dw_acc + jnp.sum(dy * xhat, axis=0, keepdims=True)
            dw_acc = dw_acc + pltpu.roll(dw_acc, HEAD_DIM, axis=1)

            @pl.when(first)
            def _(dw_ref=dw_ref, dw_acc=dw_acc):
                dw_ref[...] = dw_acc

            @pl.when(jnp.logical_not(first))
            def _(dw_ref=dw_ref, dw_acc=dw_acc):
                dw_ref[...] += dw_acc
        out_ref[:, 2 * width: 3 * width] = dv_ref[...].astype(BF16)
        out_ref[:, 3 * width: 3 * width + ch] = da_ref[...]
        out_ref[:, 3 * width + ch: out_w] = db_ref[...]

    vec = pl.BlockSpec((1, LANES), lambda i: (0, 0))
    blk = lambda c: pl.BlockSpec((tm, width), lambda i: (i, c))
    cblk = pl.BlockSpec((tm, ch), lambda i: (i, 0))
    return _pallas(
        body, name=name, grid=(s // tm,),
        in_specs=[ANY, blk(0), blk(0), blk(0), cblk, cblk, blk(0), blk(1),
                  pl.BlockSpec((tm, 1), lambda i: (i, 0)), vec, vec, vec, vec, vec,
                  pl.BlockSpec((LANES, LANES), lambda i: (0, 0))],
        out_specs=[pl.BlockSpec((tm, out_w), lambda i: (i, 0)), vec, vec],
        out_shape=[jax.ShapeDtypeStruct(dproj.shape, BF16)] + [jax.ShapeDtypeStruct((1, LANES), F32)] * 2,
        operands=[dproj, dqn, dkn, dv, da, db, proj, proj, pos_col, qw2, kw2, invf, m_a, m_b, bd],
        aliases={0: 0}, rider=rider)


def _row_chunks(n_rows, fn, chunk=256):
    def step(i, c):
        fn(pl.ds(pl.multiple_of(i * chunk, chunk), chunk))
        return c
    lax.fori_loop(0, n_rows // chunk, step, 0)


def _to_residue_major(dst, src, s, d, dst_off=0, cast=None):
    seq = s // d
    for r in range(d):
        v = src[...] if d == 1 else src[pl.ds(r, seq, stride=d), :]
        dst[dst_off + r * seq: dst_off + (r + 1) * seq, :] = v if cast is None else v.astype(cast)


def _from_residue_major(dst, src, s, d, src_off=0):
    seq = s // d
    for r in range(d):
        v = src[src_off + r * seq: src_off + (r + 1) * seq, :]
        if d == 1:
            dst[...] = v
        else:
            dst[pl.ds(r, seq, stride=d), :] = v


def _band_bias():
    qi = lax.broadcasted_iota(jnp.int32, (QBLK, KWIN), 0)
    kj = lax.broadcasted_iota(jnp.int32, (QBLK, KWIN), 1)
    return jnp.where(jnp.abs(kj - HALF_SPAN - qi) <= HALF_SPAN, 0.0, NEG_INF).astype(F32)


def _range_bias(base, seq):
    kj = lax.broadcasted_iota(jnp.int32, (1, KWIN), 1)
    lo = (base & -seq) - base + HALF_SPAN
    return jnp.where((kj >= lo) & (kj < lo + seq), 0.0, NEG_INF).astype(F32)


def _block_base(b):
    return b * QBLK if isinstance(b, int) else pl.multiple_of(b * QBLK, QBLK)


def _attn_fwd(qn, kn, proj, name, rider=None):
    s = qn.shape[0]
    n_pairs = N_SLOT_HEADS * HEAD_DIM // LANES
    v_col0 = 2 * qn.shape[1] // LANES
    nt_dims = (((1,), (1,)), ((), ()))

    def body(q_ref, k_ref, v_ref, attn_ref, lse_ref, q_rm, k_rm, v_rm, acc_rm, m_rm, l_rm,
             acc_p, m_p, l_p, m_run, l_run, acc_run, band, s_buf, m_buf):
        g = pl.program_id(1)
        zpad = jnp.zeros((HALF_SPAN, LANES), BF16)
        k_rm[0:HALF_SPAN, :] = zpad
        k_rm[s + HALF_SPAN: s + 2 * HALF_SPAN, :] = zpad
        v_rm[0:HALF_SPAN, 0:LANES] = zpad
        v_rm[s + HALF_SPAN: s + 2 * HALF_SPAN, 0:LANES] = zpad

        def ones_rows(rows):
            v_rm[pl.ds(rows.start, rows.size), LANES:2 * LANES] = jnp.ones((rows.size, LANES), BF16)

        _row_chunks(s + 2 * HALF_SPAN, ones_rows, chunk=2 * HALF_SPAN)
        band[...] = _band_bias()
        lane = lax.broadcasted_iota(jnp.int32, (QBLK, LANES), 1)
        low = lane < HEAD_DIM
        n_blk = s // QBLK

        for gi, d in enumerate(DILATIONS):
            @pl.when(g == gi)
            def _(gi=gi, d=d):
                seq = s // d
                _to_residue_major(q_rm, q_ref, s, d, cast=BF16)
                _to_residue_major(k_rm, k_ref, s, d, dst_off=HALF_SPAN, cast=BF16)
                _to_residue_major(v_rm.at[:, 0:LANES], v_ref, s, d, dst_off=HALF_SPAN, cast=BF16)

                def scores(b, slot):
                    base = _block_base(b)
                    q = q_rm[pl.ds(base, QBLK), :]
                    zero = jnp.zeros_like(q)
                    q2 = jnp.concatenate([jnp.where(low, q, zero), jnp.where(low, zero, q)], axis=0)
                    sc = lax.dot_general(q2, k_rm[pl.ds(base, KWIN), :], nt_dims, preferred_element_type=F32)
                    bias = band[...] + _range_bias(base, seq)
                    for hh in range(2):
                        rows = slice(hh * QBLK, (hh + 1) * QBLK)
                        sh = sc[rows, :] + bias
                        s_buf[slot, rows, :] = sh
                        m_buf[slot, rows, :] = jnp.broadcast_to(jnp.max(sh, axis=-1, keepdims=True), (QBLK, LANES))

                def outputs(b, slot):
                    base = _block_base(b)
                    sv = s_buf[slot]
                    mb = m_buf[slot]
                    p = jnp.exp(jnp.concatenate([sv[:, 0:LANES] - mb, sv[:, LANES:2 * LANES] - mb], axis=1))
                    pv = jnp.dot(p.astype(BF16), v_rm[pl.ds(base, KWIN), :], preferred_element_type=F32)
                    rows = pl.ds(base, QBLK)
                    acc_rm[rows, :] = jnp.where(low, pv[0:QBLK, 0:LANES], pv[QBLK:2 * QBLK, 0:LANES])
                    l_rm[rows, :] = jnp.where(low, pv[0:QBLK, LANES:2 * LANES], pv[QBLK:2 * QBLK, LANES:2 * LANES])
                    m_rm[rows, :] = jnp.where(low, mb[0:QBLK, :], mb[QBLK:2 * QBLK, :])

                scores(0, 0)

                def pair(i, carry):
                    b = 2 * i
                    outputs(b, 0)
                    scores(b + 1, 1)
                    outputs(b + 1, 1)
                    scores(b + 2, 0)
                    return carry

                lax.fori_loop(0, n_blk // 2 - 1, pair, 0)
                outputs(n_blk - 2, 0)
                scores(n_blk - 1, 1)
                outputs(n_blk - 1, 1)
                if d == 1:
                    src = (acc_rm, m_rm, l_rm)
                else:
                    for dst_, src_ in ((acc_p, acc_rm), (m_p, m_rm), (l_p, l_rm)):
                        _from_residue_major(dst_, src_, s, d)
                    src = (acc_p, m_p, l_p)

                def combine(rows):
                    a_g, m_g, l_g = src[0][rows, :], src[1][rows, :], src[2][rows, :]
                    if gi == 0:
                        m_new, l_new, a_new = m_g, l_g, a_g
                    else:
                        m_old = m_run[rows, :]
                        m_new = jnp.maximum(m_old, m_g)
                        w_old = jnp.exp(m_old - m_new)
                        w_g = jnp.exp(m_g - m_new)
                        l_new = l_run[rows, :] * w_old + l_g * w_g
                        a_new = acc_run[rows, :] * w_old + a_g * w_g
                    if gi == len(DILATIONS) - 1:
                        attn_ref[rows, :] = a_new / l_new
                        lse_ref[rows, :] = m_new + jnp.log(l_new)
                    else:
                        m_run[rows, :] = m_new
                        l_run[rows, :] = l_new
                        acc_run[rows, :] = a_new

                _row_chunks(s, combine)

    qk_spec = pl.BlockSpec((s, LANES), lambda hp, g: (0, g * n_pairs + hp))
    v_spec = pl.BlockSpec((s, LANES), lambda hp, g: (0, v_col0 + g * n_pairs + hp))
    o_spec = pl.BlockSpec((s, LANES), lambda hp, g: (0, hp))
    f32buf = pltpu.VMEM((s, LANES), F32)
    return _pallas(
        body, name=name, grid=(n_pairs, len(DILATIONS)), in_specs=[qk_spec, qk_spec, v_spec],
        out_specs=[o_spec, o_spec],
        out_shape=[jax.ShapeDtypeStruct((s, n_pairs * LANES), F32)] * 2,
        operands=[qn, kn, proj],
        scratch_shapes=[pltpu.VMEM((s, LANES), BF16), pltpu.VMEM((s + 2 * HALF_SPAN, LANES), BF16),
                        pltpu.VMEM((s + 2 * HALF_SPAN, 2 * LANES), BF16)] + [f32buf] * 9
        + [pltpu.VMEM((QBLK, KWIN), F32), pltpu.VMEM((2, 2 * QBLK, KWIN), F32),
           pltpu.VMEM((2, 2 * QBLK, LANES), F32)],
        rider=rider)


def _attn_bwd(qn, kn, proj, dattn, attn, lse, bd, name, rider=None):
    s = qn.shape[0]
    n_pairs = N_SLOT_HEADS * HEAD_DIM // LANES
    v_col0 = 2 * qn.shape[1] // LANES
    nt_dims = (((1,), (1,)), ((), ()))
    tn_dims = (((0,), (0,)), ((), ()))
    spad = s + 2 * HALF_SPAN

    def body(q_ref, k_ref, v_ref, do_ref, o_ref, lse_ref, bd_ref, dq_ref, dk_ref, dv_ref,
             q_rm, k_rm, v_rm, do_rm, lse0_rm, lse1_rm, dd0_rm, dd1_rm, dq_rm, dk_rm, dv_rm,
             lse0_p, lse1_p, dd0_p, dd1_p, band, p_buf, ds_buf):
        g = pl.program_id(1)
        zpad = jnp.zeros((HALF_SPAN, LANES), BF16)
        for buf in (k_rm, v_rm):
            buf[0:HALF_SPAN, :] = zpad
            buf[s + HALF_SPAN: spad, :] = zpad
        zf = jnp.zeros((HALF_SPAN, LANES), F32)
        for buf in (dk_rm, dv_rm):
            buf[0:HALF_SPAN, :] = zf
            buf[s + HALF_SPAN: spad, :] = zf
        band[...] = _band_bias()

        def clear(rows):
            z = jnp.zeros((rows.size, LANES), F32)
            dk_rm[pl.ds(rows.start + HALF_SPAN, rows.size), :] = z
            dv_rm[pl.ds(rows.start + HALF_SPAN, rows.size), :] = z

        _row_chunks(s, clear)

        def prepare(rows):
            lo = lax.broadcasted_iota(jnp.int32, (rows.size, LANES), 1) < HEAD_DIM
            dsum = _head_sums(do_ref[rows, :] * o_ref[rows, :], bd_ref[...])
            dswap = pltpu.roll(dsum, HEAD_DIM, axis=1)
            dd0_p[rows, :] = jnp.where(lo, dsum, dswap)
            dd1_p[rows, :] = jnp.where(lo, dswap, dsum)
            lv = lse_ref[rows, :]
            lswap = pltpu.roll(lv, HEAD_DIM, axis=1)
            lse0_p[rows, :] = jnp.where(lo, lv, lswap)
            lse1_p[rows, :] = jnp.where(lo, lswap, lv)

        @pl.when(g == 0)
        def _():
            _row_chunks(s, prepare)
        lane = lax.broadcasted_iota(jnp.int32, (QBLK, LANES), 1)
        low = lane < HEAD_DIM
        n_blk = s // QBLK

        def stacked(ref, rows):
            val = ref[rows, :]
            zero = jnp.zeros_like(val)
            return jnp.concatenate([jnp.where(low, val, zero), jnp.where(low, zero, val)], axis=0)

        for gi, d in enumerate(DILATIONS):
            @pl.when(g == gi)
            def _(d=d):
                seq = s // d
                _to_residue_major(q_rm, q_ref, s, d, cast=BF16)
                _to_residue_major(k_rm, k_ref, s, d, dst_off=HALF_SPAN, cast=BF16)
                _to_residue_major(v_rm, v_ref, s, d, dst_off=HALF_SPAN, cast=BF16)
                _to_residue_major(do_rm, do_ref, s, d, cast=BF16)
                for dst_, src_ in ((lse0_rm, lse0_p), (lse1_rm, lse1_p), (dd0_rm, dd0_p), (dd1_rm, dd1_p)):
                    _to_residue_major(dst_, src_, s, d)

                def scores(b, slot):
                    base = _block_base(b)
                    rows = pl.ds(base, QBLK)
                    win = pl.ds(base, KWIN)
                    sc = lax.dot_general(stacked(q_rm, rows), k_rm[win, :], nt_dims, preferred_element_type=F32)
                    dp = lax.dot_general(stacked(do_rm, rows), v_rm[win, :], nt_dims, preferred_element_type=F32)
                    bias = band[...] + _range_bias(base, seq)
                    for hh, (lse_r, dd_r) in enumerate(((lse0_rm, dd0_rm), (lse1_rm, dd1_rm))):
                        r = slice(hh * QBLK, (hh + 1) * QBLK)
                        lse_h = lse_r[rows, :]
                        dd_h = dd_r[rows, :]
                        sh = sc[r, :] + bias
                        p = jnp.exp(jnp.concatenate([sh[:, 0:LANES] - lse_h, sh[:, LANES:KWIN] - lse_h], axis=1))
                        dph = dp[r, :]
                        ds = p * jnp.concatenate([dph[:, 0:LANES] - dd_h, dph[:, LANES:KWIN] - dd_h], axis=1)
                        p_buf[slot, r, :] = p.astype(BF16)
                        ds_buf[slot, r, :] = ds.astype(BF16)

                def grads(b, slot):
                    base = _block_base(b)
                    rows = pl.ds(base, QBLK)
                    win = pl.ds(base, KWIN)
                    p = p_buf[slot]
                    ds = ds_buf[slot]
                    dq2 = jnp.dot(ds, k_rm[win, :], preferred_element_type=F32)
                    dq_rm[rows, :] = jnp.where(low, dq2[0:QBLK, :], dq2[QBLK:2 * QBLK, :])
                    dk_rm[win, :] += lax.dot_general(ds, stacked(q_rm, rows), tn_dims, preferred_element_type=F32)
                    dv_rm[win, :] += lax.dot_general(p, stacked(do_rm, rows), tn_dims, preferred_element_type=F32)

                scores(0, 0)

                def pair(i, carry):
                    b = 2 * i
                    grads(b, 0)
                    scores(b + 1, 1)
                    grads(b + 1, 1)
                    scores(b + 2, 0)
                    return carry

                lax.fori_loop(0, n_blk // 2 - 1, pair, 0)
                grads(n_blk - 2, 0)
                scores(n_blk - 1, 1)
                grads(n_blk - 1, 1)
                _from_residue_major(dq_ref, dq_rm, s, d)
                _from_residue_major(dk_ref, dk_rm, s, d, src_off=HALF_SPAN)
                _from_residue_major(dv_ref, dv_rm, s, d, src_off=HALF_SPAN)

    qk_spec = pl.BlockSpec((s, LANES), lambda hp, g: (0, g * n_pairs + hp))
    v_spec = pl.BlockSpec((s, LANES), lambda hp, g: (0, v_col0 + g * n_pairs + hp))
    o_spec = pl.BlockSpec((s, LANES), lambda hp, g: (0, hp))
    width = qn.shape[1]
    f32buf = pltpu.VMEM((s, LANES), F32)
    f32pad = pltpu.VMEM((spad, LANES), F32)
    return _pallas(
        body, name=name, grid=(n_pairs, len(DILATIONS)),
        in_specs=[qk_spec, qk_spec, v_spec, o_spec, o_spec, o_spec,
                  pl.BlockSpec((LANES, LANES), lambda hp, g: (0, 0))],
        out_specs=[qk_spec, qk_spec, qk_spec],
        out_shape=[jax.ShapeDtypeStruct((s, width), F32)] * 3,
        operands=[qn, kn, proj, dattn, attn, lse, bd],
        scratch_shapes=[pltpu.VMEM((s, LANES), BF16), pltpu.VMEM((spad, LANES), BF16),
                        pltpu.VMEM((spad, LANES), BF16), pltpu.VMEM((s, LANES), BF16),
                        f32buf, f32buf, f32buf, f32buf, f32buf, f32pad, f32pad,
                        f32buf, f32buf, f32buf, f32buf, pltpu.VMEM((QBLK, KWIN), F32),
                        pltpu.VMEM((2, 2 * QBLK, KWIN), BF16), pltpu.VMEM((2, 2 * QBLK, KWIN), BF16)],
        rider=rider)


CONV_PAD = 16


def _conv_fwd(proj, conv_w, conv_b, col0, name, rider=None):
    s = proj.shape[0]
    ch = conv_w.shape[1]
    nblk = ch // LANES
    a0 = col0 // LANES
    tr = 256
    shift = CONV_PAD - (CONV_WIDTH - 1) // 2

    def body(a_ref, b_ref, w_ref, bias_ref, u0_ref, uc_ref, pad):
        z = jnp.zeros((CONV_PAD, LANES), F32)
        pad[0:CONV_PAD, :] = z
        pad[s + CONV_PAD: s + 2 * CONV_PAD, :] = z

        def glu(rows):
            u0 = a_ref[rows, :] * jax.nn.sigmoid(b_ref[rows, :])
            u0_ref[rows, :] = u0
            pad[pl.ds(rows.start + CONV_PAD, rows.size), :] = u0

        _row_chunks(s, glu)
        for t in range(0, s, tr):
            acc = jnp.broadcast_to(bias_ref[...], (tr, LANES))
            for k in range(CONV_WIDTH):
                acc = acc + w_ref[k:k + 1, :] * pad[t + k + shift: t + k + shift + tr, :]
            uc_ref[t:t + tr, :] = acc

    return _pallas(
        body, name=name, grid=(nblk,),
        in_specs=[pl.BlockSpec((s, LANES), lambda c: (0, a0 + c)),
                  pl.BlockSpec((s, LANES), lambda c: (0, a0 + nblk + c)),
                  pl.BlockSpec((CONV_WIDTH, LANES), lambda c: (0, c)),
                  pl.BlockSpec((1, LANES), lambda c: (0, c))],
        out_specs=[pl.BlockSpec((s, LANES), lambda c: (0, c))] * 2,
        out_shape=[jax.ShapeDtypeStruct((s, ch), F32)] * 2, operands=[proj, proj, conv_w, conv_b],
        scratch_shapes=[pltpu.VMEM((s + 2 * CONV_PAD, LANES), F32)], rider=rider)


def _ln_silu_fwd(uc, ln_w, ln_b, name):
    s, ch = uc.shape
    tm = 256

    def body(u_ref, w_ref, b_ref, o_ref):
        u = u_ref[...]
        mu = jnp.mean(u, axis=-1, keepdims=True)
        xc = u - mu
        rstd = lax.rsqrt(jnp.mean(xc * xc, axis=-1, keepdims=True) + EPS)
        z = xc * rstd * w_ref[...] + b_ref[...]
        o_ref[...] = (z * jax.nn.sigmoid(z)).astype(BF16)

    row = pl.BlockSpec((tm, ch), lambda i: (i, 0))
    vec = pl.BlockSpec((1, ch), lambda i: (0, 0))
    return pl.pallas_call(
        body, name=name, grid=(s // tm,), in_specs=[row, vec, vec], out_specs=row,
        out_shape=jax.ShapeDtypeStruct((s, ch), BF16), compiler_params=_params(),
    )(uc, ln_w, ln_b)


def _ln_silu_bwd(du3, uc, ln_w, ln_b, name):
    s, ch = uc.shape
    tm = 256

    def body(d_ref, u_ref, w_ref, b_ref, du_ref, dw_ref, db_ref):
        u = u_ref[...]
        mu = jnp.mean(u, axis=-1, keepdims=True)
        xc = u - mu
        rstd = lax.rsqrt(jnp.mean(xc * xc, axis=-1, keepdims=True) + EPS)
        xhat = xc * rstd
        z = xhat * w_ref[...] + b_ref[...]
        sg = jax.nn.sigmoid(z)
        dz = d_ref[...] * (sg * (1.0 + z * (1.0 - sg)))
        dxh = dz * w_ref[...]
        du_ref[...] = rstd * (dxh - jnp.mean(dxh, axis=-1, keepdims=True)
                              - xhat * jnp.mean(dxh * xhat, axis=-1, keepdims=True))
        pw = jnp.sum(dz * xhat, axis=0, keepdims=True)
        pb = jnp.sum(dz, axis=0, keepdims=True)
        first = pl.program_id(0) == 0

        @pl.when(first)
        def _():
            dw_ref[...] = pw
            db_ref[...] = pb

        @pl.when(jnp.logical_not(first))
        def _():
            dw_ref[...] += pw
            db_ref[...] += pb

    row = pl.BlockSpec((tm, ch), lambda i: (i, 0))
    vec = pl.BlockSpec((1, ch), lambda i: (0, 0))
    return pl.pallas_call(
        body, name=name, grid=(s // tm,), in_specs=[row, row, vec, vec], out_specs=[row, vec, vec],
        out_shape=[jax.ShapeDtypeStruct((s, ch), F32), jax.ShapeDtypeStruct((1, ch), F32),
                   jax.ShapeDtypeStruct((1, ch), F32)],
        compiler_params=_params(),
    )(du3, uc, ln_w, ln_b)


def _conv_bwd(duc, u0, proj, conv_w, col0, name, rider=None):
    s = proj.shape[0]
    ch = conv_w.shape[1]
    nblk = ch // LANES
    a0 = col0 // LANES
    tr = 256
    half = (CONV_WIDTH - 1) // 2
    shift = CONV_PAD - half

    def body(duc_ref, u0_ref, a_ref, b_ref, w_ref, da_ref, db_ref, dw_ref, dbias_ref, pad_d, pad_u):
        z = jnp.zeros((CONV_PAD, LANES), F32)
        for buf in (pad_d, pad_u):
            buf[0:CONV_PAD, :] = z
            buf[s + CONV_PAD: s + 2 * CONV_PAD, :] = z

        def fill(rows):
            dst = pl.ds(rows.start + CONV_PAD, rows.size)
            pad_d[dst, :] = duc_ref[rows, :]
            pad_u[dst, :] = u0_ref[rows, :]

        _row_chunks(s, fill)
        dw_acc = [jnp.zeros((8, LANES), F32) for _ in range(CONV_WIDTH)]
        dbias_acc = jnp.zeros((8, LANES), F32)
        for t in range(0, s, tr):
            d_t = duc_ref[t:t + tr, :]
            dbias_acc = dbias_acc + jnp.sum(d_t.reshape(tr // 8, 8, LANES), axis=0)
            du0 = jnp.zeros((tr, LANES), F32)
            for k in range(CONV_WIDTH):
                du0 = du0 + w_ref[k:k + 1, :] * pad_d[t - k + half + CONV_PAD: t - k + half + CONV_PAD + tr, :]
                prod = d_t * pad_u[t + k + shift: t + k + shift + tr, :]
                dw_acc[k] = dw_acc[k] + jnp.sum(prod.reshape(tr // 8, 8, LANES), axis=0)
            av = a_ref[t:t + tr, :]
            sg = jax.nn.sigmoid(b_ref[t:t + tr, :])
            da_ref[t:t + tr, :] = (du0 * sg).astype(BF16)
            db_ref[t:t + tr, :] = (du0 * av * sg * (1.0 - sg)).astype(BF16)
        for k in range(CONV_WIDTH):
            dw_ref[k:k + 1, :] = jnp.sum(dw_acc[k], axis=0, keepdims=True)
        dbias_ref[...] = jnp.sum(dbias_acc, axis=0, keepdims=True)

    col = lambda off: pl.BlockSpec((s, LANES), lambda c: (0, off + c))
    return _pallas(
        body, name=name, grid=(nblk,),
        in_specs=[col(0), col(0), col(a0), col(a0 + nblk),
                  pl.BlockSpec((CONV_WIDTH, LANES), lambda c: (0, c))],
        out_specs=[col(0), col(0), pl.BlockSpec((CONV_WIDTH, LANES), lambda c: (0, c)),
                   pl.BlockSpec((1, LANES), lambda c: (0, c))],
        out_shape=[jax.ShapeDtypeStruct((s, ch), BF16)] * 2
        + [jax.ShapeDtypeStruct((CONV_WIDTH, ch), F32), jax.ShapeDtypeStruct((1, ch), F32)],
        operands=[duc, u0, proj, proj, conv_w],
        scratch_shapes=[pltpu.VMEM((s + 2 * CONV_PAD, LANES), F32)] * 2, rider=rider)


GATE_BLK = 512


def _gate_fwd(proj, bg, y_a, y_b, col0, name):
    s, d = y_a.shape
    tm = 256
    g0 = col0 // GATE_BLK
    nb = d // GATE_BLK

    def body(ga_ref, gb_ref, ba_ref, bb_ref, ya_ref, yb_ref, o_ref):
        g_a = jax.nn.sigmoid(ga_ref[...] + ba_ref[...])
        g_b = jax.nn.sigmoid(gb_ref[...] + bb_ref[...])
        o_ref[...] = (g_a * ya_ref[...] + g_b * yb_ref[...]).astype(BF16)

    act = pl.BlockSpec((tm, GATE_BLK), lambda i, j: (i, j))
    return pl.pallas_call(
        body, name=name, grid=(s // tm, nb),
        in_specs=[pl.BlockSpec((tm, GATE_BLK), lambda i, j: (i, g0 + j)),
                  pl.BlockSpec((tm, GATE_BLK), lambda i, j: (i, g0 + nb + j)),
                  pl.BlockSpec((None, 1, GATE_BLK), lambda i, j: (0, 0, j)),
                  pl.BlockSpec((None, 1, GATE_BLK), lambda i, j: (1, 0, j)), act, act],
        out_specs=act, out_shape=jax.ShapeDtypeStruct((s, d), BF16), compiler_params=_params(),
    )(proj, proj, bg, bg, y_a, y_b)


def _gate_bwd(d_mixed, proj, bg, y_a, y_b, col0, name, rider=None):
    s, d = y_a.shape
    tm = 256
    g0 = col0 // GATE_BLK
    nb = d // GATE_BLK

    def body(dm_ref, gl_ref, b_ref, y_ref, dgl_ref, dy_ref, db_ref):
        gate = jax.nn.sigmoid(gl_ref[...] + b_ref[...])
        dm = dm_ref[...]
        dy_ref[...] = (dm * gate).astype(BF16)
        dgl = dm * y_ref[...] * gate * (1.0 - gate)
        dgl_ref[...] = dgl.astype(BF16)
        part = jnp.sum(dgl, axis=0, keepdims=True)
        first = pl.program_id(1) == 0

        @pl.when(first)
        def _():
            db_ref[...] = part

        @pl.when(jnp.logical_not(first))
        def _():
            db_ref[...] += part

    def call(y, branch, dproj_prev, rider):
        def wrapped(*refs):
            if dproj_prev is not None:
                refs = refs[1:]
            body(*refs)

        ins = [pl.BlockSpec((tm, GATE_BLK), lambda j, i: (i, j)),
               pl.BlockSpec((tm, GATE_BLK), lambda j, i: (i, g0 + branch * nb + j)),
               pl.BlockSpec((None, 1, GATE_BLK), lambda j, i: (branch, 0, j)),
               pl.BlockSpec((tm, GATE_BLK), lambda j, i: (i, j))]
        ops = [d_mixed, proj, bg, y]
        alias = {}
        if dproj_prev is not None:
            ins = [ANY] + ins
            ops = [dproj_prev] + ops
            alias = {0: 0}
        return _pallas(
            wrapped, name=f"{name}_{branch}", grid=(nb, s // tm), in_specs=ins,
            out_specs=[pl.BlockSpec((tm, GATE_BLK), lambda j, i: (i, g0 + branch * nb + j)),
                       pl.BlockSpec((tm, GATE_BLK), lambda j, i: (i, j)),
                       pl.BlockSpec((1, GATE_BLK), lambda j, i: (0, j))],
            out_shape=[jax.ShapeDtypeStruct((s, proj.shape[1]), BF16), jax.ShapeDtypeStruct((s, d), BF16),
                       jax.ShapeDtypeStruct((1, d), F32)],
            operands=ops, aliases=alias, rider=rider)

    first = call(y_a, 0, None, rider)
    (dproj, dy_a, db_a), rider_out = first if rider is not None else (first, None)
    dproj, dy_b, db_b = call(y_b, 1, dproj, None)
    res = (dproj, dy_a, dy_b, db_a, db_b)
    return res if rider is None else (res, rider_out)


def _swiglu_fwd(gu, name):
    s, w2 = gu.shape
    ff = w2 // 2
    tm = 256
    tn = ff // 2

    def body(g_ref, u_ref, o_ref):
        gt = g_ref[...]
        o_ref[...] = (gt * jax.nn.sigmoid(gt) * u_ref[...]).astype(BF16)

    return pl.pallas_call(
        body, name=name, grid=(s // tm, 2),
        in_specs=[pl.BlockSpec((tm, tn), lambda i, j: (i, j)), pl.BlockSpec((tm, tn), lambda i, j: (i, 2 + j))],
        out_specs=pl.BlockSpec((tm, tn), lambda i, j: (i, j)),
        out_shape=jax.ShapeDtypeStruct((s, ff), BF16), compiler_params=_params(),
    )(gu, gu)


def _swiglu_bwd(gu, d_act, name, rider=None):
    s, w2 = gu.shape
    ff = w2 // 2
    tm = 256
    tn = ff // 2

    def body(g_ref, u_ref, d_ref, o_ref):
        gt = g_ref[...]
        sg = jax.nn.sigmoid(gt)
        dv = d_ref[...]
        is_gate = pl.program_id(1) < 2

        @pl.when(is_gate)
        def _():
            o_ref[...] = (dv * u_ref[...] * (sg * (1.0 + gt * (1.0 - sg)))).astype(BF16)

        @pl.when(jnp.logical_not(is_gate))
        def _():
            o_ref[...] = (dv * gt * sg).astype(BF16)

    return _pallas(
        body, name=name, grid=(s // tm, 4),
        in_specs=[pl.BlockSpec((tm, tn), lambda i, j: (i, j % 2)),
                  pl.BlockSpec((tm, tn), lambda i, j: (i, 2 + j % 2)),
                  pl.BlockSpec((tm, tn), lambda i, j: (i, j % 2))],
        out_specs=pl.BlockSpec((tm, tn), lambda i, j: (i, j)),
        out_shape=jax.ShapeDtypeStruct((s, w2), BF16), operands=[gu, gu, d_act], rider=rider)


def _loss_fwd_bwd(y, target, name):
    s, d = y.shape
    tm = 256

    def body(y_ref, t_ref, dy_ref, dyb_ref, loss_ref, acc):
        diff = y_ref[...] - t_ref[...]
        dy = diff * (1.0 / d)
        dy_ref[...] = dy
        dyb_ref[...] = dy.astype(BF16)
        part = jnp.sum((diff * diff).reshape(tm // 8, 8, d), axis=0)
        i = pl.program_id(0)

        @pl.when(i == 0)
        def _():
            acc[...] = part

        @pl.when(i > 0)
        def _():
            acc[...] += part

        @pl.when(i == pl.num_programs(0) - 1)
        def _():
            loss_ref[...] = (0.5 / d) * jnp.sum(jnp.sum(acc[...], axis=1, keepdims=True), axis=0, keepdims=True)

    row = pl.BlockSpec((tm, d), lambda i: (i, 0))
    return pl.pallas_call(
        body, name=name, grid=(s // tm,), in_specs=[row, row],
        out_specs=[row, row, pl.BlockSpec((1, 1), lambda i: (0, 0))],
        out_shape=[jax.ShapeDtypeStruct((s, d), F32), jax.ShapeDtypeStruct((s, d), BF16),
                   jax.ShapeDtypeStruct((1, 1), F32)],
        scratch_shapes=[pltpu.VMEM((8, d), F32)], compiler_params=_params(),
    )(y, target)


LATE_GATHER = ("w_o_attn", "w_pw_conv", "w_out", "w_ffn_in", "w_ffn_out")
EARLY_REDUCE = LATE_GATHER


def _blocks_by_half(g):
    if g.ndim == 2:
        g = g.reshape(N_CHIPS, g.shape[0] // N_CHIPS, g.shape[1])
    return g.reshape(N_CHIPS, 2, g.shape[1] // 2, g.shape[2])


def _forward_backward(x, pos_col, target, wts, late_bufs, pos_arr):
    wts = dict(wts)
    consts = _rope_consts()
    bd = consts[3]
    qw2 = jnp.tile(wts["q_norm_w"], (1, LANES // HEAD_DIM))
    kw2 = jnp.tile(wts["k_norm_w"], (1, LANES // HEAD_DIM))
    qkv_w = 3 * N_SLOT_HEADS * HEAD_DIM
    conv_col0 = 3 * qkv_w
    ch = wts["conv_w"].shape[1]
    gate_col0 = conv_col0 + 2 * ch

    h = _rmsnorm_fwd(x, wts["norm1_w"], "rms1_fwd")
    late = dict(zip(LATE_GATHER, late_bufs))
    half_rows = late["w_ffn_in"].shape[2]
    proj, (late["w_ffn_out"], late["w_out"], late["w_o_attn"]) = _matmul(
        h, wts["w_in"], mode="nn", tm=1024, tn=640, tk=1024, out_dtype=F32, name="mm_proj", b_blocked=True,
        rider=_gather_ici_rider([late["w_ffn_out"], late["w_out"], late["w_o_attn"]], []))
    (qn, kn), (late["w_pw_conv"], late["w_ffn_in"]) = _qk_fwd(
        proj, pos_col, qw2, kw2, consts, "qk_fwd",
        rider=_gather_ici_rider([late["w_pw_conv"], late["w_ffn_in"]], [],
                                row_ranges=[None, (0, half_rows // 2)]))
    (attn, lse), (late["w_ffn_in"],) = _attn_fwd(
        qn, kn, proj, "attn_fwd",
        rider=_gather_ici_rider([late["w_ffn_in"]], [], row_ranges=[(half_rows // 2, half_rows // 2)]))
    (u0, uc), late_bufs = _conv_fwd(proj, wts["conv_w"], wts["conv_b"], conv_col0, "conv_fwd",
                                    rider=_gather_forward_rider([late[n] for n in LATE_GATHER]))
    for n, buf in zip(LATE_GATHER, late_bufs):
        full = buf.reshape(N_CHIPS, -1, buf.shape[3])
        wts[n] = full.reshape(-1, full.shape[2]) if n in ROW_SHARDED else full
    attn_b = attn.astype(BF16)
    y_a = _matmul(attn_b, wts["w_o_attn"], mode="nn", tm=1024, tn=256, tk=512, out_dtype=F32, name="mm_ya",
                  b_blocked=True)
    u3 = _ln_silu_fwd(uc, wts["conv_ln_w"], wts["conv_ln_b"], "ln_fwd")
    y_b = _matmul(u3, wts["w_pw_conv"], mode="nn", tm=1024, tn=256, tk=512, out_dtype=F32, name="mm_yb",
                  b_blocked=True)
    mixed = _gate_fwd(proj, wts["b_gate"], y_a, y_b, gate_col0, "gate_fwd")
    x1 = _matmul(mixed, wts["w_out"], mode="nn", tm=1024, tn=512, tk=1024, out_dtype=F32, name="mm_x1",
                 residual=x)
    h2 = _rmsnorm_fwd(x1, wts["norm2_w"], "rms2_fwd")
    gu = _matmul(h2, wts["w_ffn_in"], mode="nn", tm=512, tn=1408, tk=1024, out_dtype=F32, name="mm_gu",
                 b_blocked=True)
    act = _swiglu_fwd(gu, "swiglu_fwd")
    x2 = _matmul(act, wts["w_ffn_out"], mode="nn", tm=512, tn=512, tk=2816, out_dtype=F32, name="mm_x2",
                 residual=x1)
    dy, dy_b16, loss = _loss_fwd_bwd(x2, target, "loss")

    g = {}
    by_chip = {}

    def pair_add(n, blocks, received):
        return _add_own_half(blocks, received, pos_arr, f"grads_pair_add_{n}")

    d_act = _matmul(dy_b16, wts["w_ffn_out"], mode="nt", tm=512, tn=1408, tk=1024, out_dtype=F32, name="mm_dact")
    g_ffn_out = _blocks_by_half(
        _matmul(act, dy_b16, mode="tn", tm=1408, tn=512, tk=2048, out_dtype=F32, name="mm_dwffnout"))
    dgu, (received,) = _swiglu_bwd(gu, d_act, "swiglu_bwd", rider=_pair_exchange_rider([g_ffn_out], halved=True))
    to_send, own = pair_add("w_ffn_out", g_ffn_out, received)
    dh2, (by_chip["w_ffn_out"],) = _matmul(
        dgu, wts["w_ffn_in"], mode="nt", tm=512, tn=1024, tk=1408, out_dtype=F32, name="mm_dh2", b_blocked=True,
        rider=_chip_exchange_rider([to_send], [own]))
    g_ffn_in = _blocks_by_half(_matmul(h2, dgu, mode="tn", tm=512, tn=1408, tk=2048, out_dtype=F32,
                                       name="mm_dwffnin", out_blocked=N_CHIPS))
    dx1, dx1_b16, g["norm2_w"] = _rmsnorm_bwd(dh2, x1, wts["norm2_w"], dy, "rms2_bwd")
    d_mixed = _matmul(dx1_b16, wts["w_out"], mode="nt", tm=1024, tn=512, tk=1024, out_dtype=F32, name="mm_dmixed")
    g["w_out"] = _matmul(mixed, dx1_b16, mode="tn", tm=512, tn=1024, tk=2048, out_dtype=F32, name="mm_dwout")
    (dproj, dy_a, dy_b, db_a, db_b), (received,) = _gate_bwd(
        d_mixed, proj, wts["b_gate"], y_a, y_b, gate_col0, "gate_bwd",
        rider=_pair_exchange_rider([g_ffn_in], halved=True))
    ffn_in_to_send, ffn_in_own = pair_add("w_ffn_in", g_ffn_in, received)
    g["b_gate"] = jnp.concatenate([db_a, db_b], axis=0)
    dattn = _matmul(dy_a, wts["w_o_attn"], mode="nt", tm=1024, tn=512, tk=256, out_dtype=F32, name="mm_dattn",
                    b_blocked=True)
    g["w_o_attn"] = _matmul(attn_b, dy_a, mode="tn", tm=512, tn=256, tk=2048, out_dtype=F32, name="mm_dwo",
                            out_blocked=N_CHIPS)
    du3 = _matmul(dy_b, wts["w_pw_conv"], mode="nt", tm=1024, tn=512, tk=256, out_dtype=F32, name="mm_du3",
                  b_blocked=True)
    g["w_pw_conv"] = _matmul(u3, dy_b, mode="tn", tm=512, tn=256, tk=2048, out_dtype=F32, name="mm_dwpw",
                             out_blocked=N_CHIPS)
    duc, g["conv_ln_w"], g["conv_ln_b"] = _ln_silu_bwd(du3, uc, wts["conv_ln_w"], wts["conv_ln_b"], "ln_bwd")

    small3 = ("w_out", "w_o_attn", "w_pw_conv")
    g_small3 = [_blocks_by_half(g.pop(n)) for n in small3]
    (da, db, g["conv_w"], g["conv_b"]), received = _conv_bwd(
        duc, u0, proj, wts["conv_w"], conv_col0, "conv_bwd", rider=_pair_exchange_rider(g_small3, halved=True))
    sums3 = [pair_add(n, gb, rv) for n, gb, rv in zip(small3, g_small3, received)]
    (dqn, dkn, dv), (by_chip["w_ffn_in"],) = _attn_bwd(
        qn, kn, proj, dattn, attn, lse, bd, "attn_bwd",
        rider=_chip_exchange_rider([ffn_in_to_send], [ffn_in_own]))
    (dproj, dqw, dkw), exchanged3 = _qk_bwd(
        dproj, dqn, dkn, dv, da, db, proj, pos_col, qw2, kw2, consts, "qk_bwd",
        rider=_chip_exchange_rider([s[0] for s in sums3], [s[1] for s in sums3]))
    by_chip.update(zip(small3, exchanged3))
    halves = [_sum_chips(by_chip[n], pos_arr, f"grads_chip_sum_{n}") for n in EARLY_REDUCE]
    g["q_norm_w"] = dqw[:, :HEAD_DIM]
    g["k_norm_w"] = dkw[:, :HEAD_DIM]

    c = pos_arr[0]
    rh = h.shape[1] // 2
    h_sibling = lax.dynamic_slice_in_dim(h, (1 - c) * rh, rh, axis=1)
    h_own = lax.dynamic_slice_in_dim(h, c * rh, rh, axis=1)
    g_sibling, shards = _matmul(h_sibling, dproj, mode="tn", tm=rh, tn=640, tk=2048, out_dtype=F32,
                                name="mm_dwin_sibling", out_blocked=N_CHIPS, rider=_pair_gather_rider(halves))
    reduced = dict(zip(EARLY_REDUCE, shards))
    g_own, from_sibling = _matmul(h_own, dproj, mode="tn", tm=rh, tn=640, tk=2048, out_dtype=F32,
                                  name="mm_dwin_own", out_blocked=N_CHIPS,
                                  rider=_pair_exchange_rider([g_sibling], halved=False))
    to_send, own = _add_own_half(g_own, from_sibling[0], pos_arr, "grads_pair_add_w_in")
    dh, by_chip_w_in = _matmul(dproj, wts["w_in"], mode="nt", tm=512, tn=1024, tk=1920, out_dtype=F32, name="mm_dh",
                               b_blocked=True, rider=_chip_exchange_rider([to_send], [own]))
    grad_x, _, g["norm1_w"] = _rmsnorm_bwd(dh, x, wts["norm1_w"], dx1, "rms1_bwd")
    return loss, grad_x, g, reduced, by_chip_w_in[0]


def _mesh_pos():
    return lax.axis_index("x"), lax.axis_index("y"), lax.axis_index("c")


def _other_chips(x, y):
    return [(1 - x, y), (x, 1 - y), (1 - x, 1 - y)]


def _cast_into_slot(shard, chip_arr, dtype, name):
    r, c = shard.shape
    tr = r // 2 if r % 32 == 0 else r

    def body(chip_ref, s_ref, o_ref):
        del chip_ref
        o_ref[...] = s_ref[...].astype(dtype)

    return pl.pallas_call(
        body, name=name,
        grid_spec=pltpu.PrefetchScalarGridSpec(
            num_scalar_prefetch=1, grid=(r // tr,),
            in_specs=[pl.BlockSpec((tr, c), lambda i, chip_ref: (i, 0))],
            out_specs=pl.BlockSpec((None, tr, c), lambda i, chip_ref: (chip_ref[0], i, 0))),
        out_shape=jax.ShapeDtypeStruct((N_CHIPS, r, c), dtype), compiler_params=_params(),
    )(chip_arr, shard)


def _allgather_inplace(big, small, name):
    nb, ns = len(big), len(small)
    n = nb + ns

    def body(*refs):
        bufs = refs[n:2 * n]
        send_sems, recv_sems, fsend_sems, frecv_sems = refs[2 * n:]
        x, y, c = _mesh_pos()
        me = 2 * x + y
        chips = _other_chips(x, y)

        def part(a, slot, half):
            return bufs[a].at[slot, half] if a < nb else bufs[a].at[slot]

        sends = []
        for a in range(n):
            for k, (px, py) in enumerate(chips):
                cp = pltpu.make_async_remote_copy(
                    src_ref=part(a, me, c), dst_ref=part(a, me, c), send_sem=send_sems.at[a, k],
                    recv_sem=recv_sems.at[a, k], device_id=(px, py, c), device_id_type=MESH)
                cp.start()
                sends.append(cp)
        for a in range(n):
            for k, (px, py) in enumerate(chips):
                slot = 2 * px + py
                pltpu.make_async_remote_copy(
                    src_ref=part(a, slot, c), dst_ref=part(a, slot, c), send_sem=send_sems.at[a, k],
                    recv_sem=recv_sems.at[a, k], device_id=(px, py, c), device_id_type=MESH).wait_recv()
                if a < nb:
                    fwd = pltpu.make_async_remote_copy(
                        src_ref=part(a, slot, c), dst_ref=part(a, slot, c), send_sem=fsend_sems.at[a, k],
                        recv_sem=frecv_sems.at[a, k], device_id=(x, y, 1 - c), device_id_type=MESH)
                    fwd.start()
                    sends.append(fwd)
        for a in range(nb):
            for k, (px, py) in enumerate(chips):
                slot = 2 * px + py
                pltpu.make_async_remote_copy(
                    src_ref=part(a, slot, 1 - c), dst_ref=part(a, slot, 1 - c), send_sem=fsend_sems.at[a, k],
                    recv_sem=frecv_sems.at[a, k], device_id=(x, y, 1 - c), device_id_type=MESH).wait_recv()
        for cp in sends:
            cp.wait_send()

    ops = list(big) + list(small)
    return pl.pallas_call(
        body, name=name, in_specs=[ANY] * n, out_specs=[ANY] * n,
        out_shape=[jax.ShapeDtypeStruct(o.shape, o.dtype) for o in ops],
        input_output_aliases={i: i for i in range(n)},
        scratch_shapes=[pltpu.SemaphoreType.DMA((n, 3)), pltpu.SemaphoreType.DMA((n, 3)),
                        pltpu.SemaphoreType.DMA((nb, 3)), pltpu.SemaphoreType.DMA((nb, 3))],
    )(*ops)


def _comm_call(rider, name):
    def body():
        pass

    return _pallas(body, name=name, grid=(1,), in_specs=[], out_specs=[], out_shape=[], operands=[],
                   rider=rider)[1]


def _gather_ici_rider(big, small, row_ranges=None):
    nb = len(big)
    n = nb + len(small)
    row_ranges = row_ranges or [None] * nb

    def copies(bufs, sems):
        x, y, c = _mesh_pos()
        me = 2 * x + y

        def part(a, slot):
            if a >= nb:
                return bufs[a].at[slot]
            if row_ranges[a] is None:
                return bufs[a].at[slot, c]
            return bufs[a].at[slot, c, pl.ds(*row_ranges[a])]
        out = []
        for a in range(n):
            for k, (px, py) in enumerate(_other_chips(x, y)):
                send = functools.partial(
                    pltpu.make_async_remote_copy,
                    src_ref=part(a, me), dst_ref=part(a, me), send_sem=sems[0].at[a, k],
                    recv_sem=sems[1].at[a, k], device_id=(px, py, c), device_id_type=MESH)
                recv = functools.partial(
                    pltpu.make_async_remote_copy,
                    src_ref=part(a, 2 * px + py), dst_ref=part(a, 2 * px + py), send_sem=sems[0].at[a, k],
                    recv_sem=sems[1].at[a, k], device_id=(px, py, c), device_id_type=MESH)
                out.append((send, recv))
        return out

    def start(r_in, r_out, sems):
        for send, _ in copies(r_out, sems):
            send().start()

    def wait(r_in, r_out, sems):
        cps = copies(r_out, sems)
        for _, recv in cps:
            recv().wait_recv()
        for send, _ in cps:
            send().wait_send()

    ops = list(big) + list(small)
    return _Rider(ops, [jax.ShapeDtypeStruct(o.shape, o.dtype) for o in ops], {i: i for i in range(n)},
                  [pltpu.SemaphoreType.DMA((n, 3)), pltpu.SemaphoreType.DMA((n, 3))], start, wait)


def _gather_forward_rider(big):
    n = len(big)

    def copies(bufs, sems):
        x, y, c = _mesh_pos()
        out = []
        for a in range(n):
            for k, (px, py) in enumerate(_other_chips(x, y)):
                slot = 2 * px + py
                send = functools.partial(
                    pltpu.make_async_remote_copy,
                    src_ref=bufs[a].at[slot, c], dst_ref=bufs[a].at[slot, c], send_sem=sems[0].at[a, k],
                    recv_sem=sems[1].at[a, k], device_id=(x, y, 1 - c), device_id_type=MESH)
                recv = functools.partial(
                    pltpu.make_async_remote_copy,
                    src_ref=bufs[a].at[slot, 1 - c], dst_ref=bufs[a].at[slot, 1 - c], send_sem=sems[0].at[a, k],
                    recv_sem=sems[1].at[a, k], device_id=(x, y, 1 - c), device_id_type=MESH)
                out.append((send, recv))
        return out

    def start(r_in, r_out, sems):
        for send, _ in copies(r_out, sems):
            send().start()

    def wait(r_in, r_out, sems):
        cps = copies(r_out, sems)
        for _, recv in cps:
            recv().wait_recv()
        for send, _ in cps:
            send().wait_send()

    return _Rider(big, [jax.ShapeDtypeStruct(o.shape, o.dtype) for o in big], {i: i for i in range(n)},
                  [pltpu.SemaphoreType.DMA((n, 3)), pltpu.SemaphoreType.DMA((n, 3))], start, wait)


def _pair_exchange_rider(gs, halved):
    n = len(gs)

    def copies(r_in, r_out, sems):
        x, y, c = _mesh_pos()
        return [pltpu.make_async_remote_copy(
            src_ref=r_in[a].at[:, 1 - c] if halved else r_in[a], dst_ref=r_out[a], send_sem=sems[0].at[a],
            recv_sem=sems[1].at[a], device_id=(x, y, 1 - c), device_id_type=MESH) for a in range(n)]

    def start(r_in, r_out, sems):
        for cp in copies(r_in, r_out, sems):
            cp.start()

    def wait(r_in, r_out, sems):
        for cp in copies(r_in, r_out, sems):
            cp.wait()

    return _Rider(gs, [jax.ShapeDtypeStruct((g.shape[0],) + g.shape[-2:], g.dtype) for g in gs], {},
                  [pltpu.SemaphoreType.DMA((n,)), pltpu.SemaphoreType.DMA((n,))], start, wait)


def _chip_exchange_rider(to_send, by_chip):
    n = len(to_send)

    def copies(r_in, r_out, sems):
        x, y, c = _mesh_pos()
        me = 2 * x + y
        out = []
        for a in range(n):
            for k, (px, py) in enumerate(_other_chips(x, y)):
                send = functools.partial(
                    pltpu.make_async_remote_copy,
                    src_ref=r_in[a].at[2 * px + py], dst_ref=r_out[a].at[me], send_sem=sems[0].at[a, k],
                    recv_sem=sems[1].at[a, k], device_id=(px, py, c), device_id_type=MESH)
                recv = functools.partial(
                    pltpu.make_async_remote_copy,
                    src_ref=r_in[a].at[me], dst_ref=r_out[a].at[2 * px + py], send_sem=sems[0].at[a, k],
                    recv_sem=sems[1].at[a, k], device_id=(px, py, c), device_id_type=MESH)
                out.append((send, recv))
        return out

    def start(r_in, r_out, sems):
        for send, _ in copies(r_in, r_out, sems):
            send().start()

    def wait(r_in, r_out, sems):
        cps = copies(r_in, r_out, sems)
        for _, recv in cps:
            recv().wait_recv()
        for send, _ in cps:
            send().wait_send()

    return _Rider(list(to_send) + list(by_chip), [jax.ShapeDtypeStruct(b.shape, b.dtype) for b in by_chip],
                  {n + i: i for i in range(n)},
                  [pltpu.SemaphoreType.DMA((n, 3)), pltpu.SemaphoreType.DMA((n, 3))], start, wait)


def _pair_gather_rider(bufs):
    n = len(bufs)

    def copies(r_out, sems):
        x, y, c = _mesh_pos()
        out = []
        for a in range(n):
            send = functools.partial(
                    pltpu.make_async_remote_copy,
                src_ref=r_out[a].at[c], dst_ref=r_out[a].at[c], send_sem=sems[0].at[a],
                recv_sem=sems[1].at[a], device_id=(x, y, 1 - c), device_id_type=MESH)
            recv = functools.partial(
                    pltpu.make_async_remote_copy,
                src_ref=r_out[a].at[1 - c], dst_ref=r_out[a].at[1 - c], send_sem=sems[0].at[a],
                recv_sem=sems[1].at[a], device_id=(x, y, 1 - c), device_id_type=MESH)
            out.append((send, recv))
        return out

    def start(r_in, r_out, sems):
        for send, _ in copies(r_out, sems):
            send().start()

    def wait(r_in, r_out, sems):
        cps = copies(r_out, sems)
        for _, recv in cps:
            recv().wait_recv()
        for send, _ in cps:
            send().wait_send()

    return _Rider(bufs, [jax.ShapeDtypeStruct(b.shape, b.dtype) for b in bufs], {i: i for i in range(n)},
                  [pltpu.SemaphoreType.DMA((n,)), pltpu.SemaphoreType.DMA((n,))], start, wait)


def _add_own_half(g, recv, pos_arr, name):
    nb, rh, cols = g.shape[0], g.shape[-2], g.shape[-1]

    def body(pos_ref, g_ref, r_ref, send_ref, own_ref):
        s = (g_ref[...] + r_ref[...]).astype(BF16)
        send_ref[...] = s

        @pl.when(pl.program_id(0) == pos_ref[1])
        def _():
            own_ref[...] = s

    blk = pl.BlockSpec((None, rh, cols), lambda j, pos_ref: (j, 0, 0))
    g_spec = blk if g.ndim == 3 else pl.BlockSpec((None, None, rh, cols),
                                                   lambda j, pos_ref: (j, pos_ref[0], 0, 0))
    shape = jax.ShapeDtypeStruct((nb, rh, cols), BF16)
    return pl.pallas_call(
        body, name=name,
        grid_spec=pltpu.PrefetchScalarGridSpec(
            num_scalar_prefetch=1, grid=(nb,), in_specs=[g_spec, blk],
            out_specs=[blk, pl.BlockSpec((None, rh, cols), lambda j, pos_ref: (pos_ref[1], 0, 0))]),
        out_shape=[shape, shape], compiler_params=_params(),
    )(pos_arr, g, recv)


def _sum_chips(gath, pos_arr, name):
    nb, rh, cols = gath.shape

    def body(pos_ref, a_ref, b_ref, c_ref, d_ref, o_ref):
        del pos_ref
        o_ref[...] = ((a_ref[...].astype(F32) + b_ref[...].astype(F32)) + c_ref[...].astype(F32)) \
            + d_ref[...].astype(F32)

    tr = rh // 2 if (rh // 2) % 16 == 0 else rh
    specs = [pl.BlockSpec((None, tr, cols), functools.partial(lambda i, pos_ref, j: (j, i, 0), j=j))
             for j in range(nb)]
    return pl.pallas_call(
        body, name=name,
        grid_spec=pltpu.PrefetchScalarGridSpec(
            num_scalar_prefetch=1, grid=(rh // tr,), in_specs=specs,
            out_specs=pl.BlockSpec((None, tr, cols), lambda i, pos_ref: (pos_ref[0], i, 0))),
        out_shape=jax.ShapeDtypeStruct((2, rh, cols), F32), compiler_params=_params(),
    )(pos_arr, gath, gath, gath, gath)


def _small_allreduce(v, name):
    n = v.shape[0]
    n_dev = 8

    def body(v_ref, o_ref, buf, send_sems, recv_sems):
        x, y, c = _mesh_pos()
        me = 4 * x + 2 * y + c
        buf[me] = v_ref[...]
        sends = []
        peers = []
        for r in range(1, n_dev):
            px = 1 - x if r & 4 else x
            py = 1 - y if r & 2 else y
            pc = 1 - c if r & 1 else c
            peers.append((px, py, pc))
            cp = pltpu.make_async_remote_copy(
                src_ref=v_ref, dst_ref=buf.at[me], send_sem=send_sems.at[r - 1],
                recv_sem=recv_sems.at[r - 1], device_id=(px, py, pc), device_id_type=MESH)
            cp.start()
            sends.append(cp)
        for r, (px, py, pc) in enumerate(peers):
            pltpu.make_async_remote_copy(
                src_ref=v_ref, dst_ref=buf.at[4 * px + 2 * py + pc], send_sem=send_sems.at[r],
                recv_sem=recv_sems.at[r], device_id=(px, py, pc), device_id_type=MESH).wait_recv()
        for cp in sends:
            cp.wait_send()
        acc = buf[0]
        for i in range(1, n_dev):
            acc = acc + buf[i]
        o_ref[...] = acc

    vm = pl.BlockSpec(memory_space=pltpu.VMEM)
    return pl.pallas_call(
        body, name=name, in_specs=[vm], out_specs=vm, out_shape=jax.ShapeDtypeStruct(v.shape, v.dtype),
        scratch_shapes=[pltpu.VMEM((n_dev, n, LANES), F32), pltpu.SemaphoreType.DMA((n_dev - 1,)),
                        pltpu.SemaphoreType.DMA((n_dev - 1,))],
        compiler_params=_params(),
    )(v)


def _adamw_math(w, g, m, v):
    m = ADAM_B1 * m + (1.0 - ADAM_B1) * g
    v = ADAM_B2 * v + (1.0 - ADAM_B2) * (g * g)
    m_hat = m / (1.0 - ADAM_B1 ** ADAM_STEP)
    v_hat = v / (1.0 - ADAM_B2 ** ADAM_STEP)
    delta = -ADAM_LR * (m_hat / (jnp.sqrt(v_hat) + ADAM_EPS) + ADAM_WD * w)
    return delta, m, v


def _adamw(w, g, m, v, name, rider=None):
    r, c = w.shape
    tr = 128 if r % 128 == 0 else 64
    assert r % tr == 0

    def body(w_ref, g_ref, m_ref, v_ref, d_ref, mo_ref, vo_ref):
        d, mn, vn = _adamw_math(w_ref[...], g_ref[...], m_ref[...], v_ref[...])
        d_ref[...] = d
        mo_ref[...] = mn
        vo_ref[...] = vn

    blk = pl.BlockSpec((tr, c), lambda i: (i, 0))
    return _pallas(body, name=name, grid=(r // tr,), in_specs=[blk] * 4, out_specs=[blk] * 3,
                   out_shape=[jax.ShapeDtypeStruct((r, c), F32)] * 3, operands=[w, g, m, v], rider=rider)


def _adamw_small(ws, gs, ms, vs, name):
    n = len(ws)

    def body(*refs):
        w_r, g_r, m_r, v_r = refs[:n], refs[n:2 * n], refs[2 * n:3 * n], refs[3 * n:4 * n]
        d_o, m_o, v_o = refs[4 * n:5 * n], refs[5 * n:6 * n], refs[6 * n:7 * n]
        for i in range(n):
            d, mn, vn = _adamw_math(w_r[i][...], g_r[i][...], m_r[i][...], v_r[i][...])
            d_o[i][...] = d
            m_o[i][...] = mn
            v_o[i][...] = vn

    vm = pl.BlockSpec(memory_space=pltpu.VMEM)
    shapes = [jax.ShapeDtypeStruct(w.shape, F32) for w in ws]
    outs = pl.pallas_call(
        body, name=name, in_specs=[vm] * (4 * n), out_specs=[vm] * (3 * n), out_shape=shapes * 3,
        compiler_params=_params(),
    )(*ws, *gs, *ms, *vs)
    return outs[:n], outs[n:2 * n], outs[2 * n:]


BIG = ("w_in", "w_o_attn", "w_pw_conv", "w_out", "w_ffn_in", "w_ffn_out")
ROW_SHARDED = ("w_out", "w_ffn_out")
SMALL = ("norm1_w", "b_gate", "q_norm_w", "k_norm_w", "conv_w", "conv_b", "conv_ln_w", "conv_ln_b", "norm2_w")
ORDER = ("norm1_w", "w_in", "b_gate", "q_norm_w", "k_norm_w", "w_o_attn", "conv_w", "conv_b", "conv_ln_w",
         "conv_ln_b", "w_pw_conv", "w_out", "norm2_w", "w_ffn_in", "w_ffn_out")
PACK_TILE = 8 * LANES


def _pack_small(parts):
    rows = []
    for p in parts:
        flat = p.reshape(-1)
        pad = (-flat.shape[0]) % PACK_TILE
        rows.append(jnp.pad(flat, (0, pad)).reshape(-1, LANES))
    return jnp.concatenate(rows, axis=0)


def _unpack_small(packed, shapes):
    out, row = [], 0
    for shp in shapes:
        size = int(np.prod(shp))
        nrow = -(-size // PACK_TILE) * (PACK_TILE // LANES)
        out.append(packed[row:row + nrow].reshape(-1)[:size].reshape(shp))
        row += nrow
    return out


def kernel(x, positions, norm1_w, w_in, b_gate, q_norm_w, k_norm_w, w_o_attn, conv_w, conv_b, conv_ln_w, conv_ln_b, w_pw_conv, w_out, norm2_w, w_ffn_in, w_ffn_out, loss_target, m_norm1_w, m_w_in, m_b_gate, m_q_norm_w, m_k_norm_w, m_w_o_attn, m_conv_w, m_conv_b, m_conv_ln_w, m_conv_ln_b, m_w_pw_conv, m_w_out, m_norm2_w, m_w_ffn_in, m_w_ffn_out, v_norm1_w, v_w_in, v_b_gate, v_q_norm_w, v_k_norm_w, v_w_o_attn, v_conv_w, v_conv_b, v_conv_ln_w, v_conv_ln_b, v_w_pw_conv, v_w_out, v_norm2_w, v_w_ffn_in, v_w_ffn_out):
    w = dict(norm1_w=norm1_w, w_in=w_in, b_gate=b_gate, q_norm_w=q_norm_w, k_norm_w=k_norm_w, w_o_attn=w_o_attn,
             conv_w=conv_w, conv_b=conv_b, conv_ln_w=conv_ln_w, conv_ln_b=conv_ln_b, w_pw_conv=w_pw_conv,
             w_out=w_out, norm2_w=norm2_w, w_ffn_in=w_ffn_in, w_ffn_out=w_ffn_out)
    m = dict(norm1_w=m_norm1_w, w_in=m_w_in, b_gate=m_b_gate, q_norm_w=m_q_norm_w, k_norm_w=m_k_norm_w,
             w_o_attn=m_w_o_attn, conv_w=m_conv_w, conv_b=m_conv_b, conv_ln_w=m_conv_ln_w,
             conv_ln_b=m_conv_ln_b, w_pw_conv=m_w_pw_conv, w_out=m_w_out, norm2_w=m_norm2_w,
             w_ffn_in=m_w_ffn_in, w_ffn_out=m_w_ffn_out)
    v = dict(norm1_w=v_norm1_w, w_in=v_w_in, b_gate=v_b_gate, q_norm_w=v_q_norm_w, k_norm_w=v_k_norm_w,
             w_o_attn=v_w_o_attn, conv_w=v_conv_w, conv_b=v_conv_b, conv_ln_w=v_conv_ln_w,
             conv_ln_b=v_conv_ln_b, w_pw_conv=v_w_pw_conv, w_out=v_w_out, norm2_w=v_norm2_w,
             w_ffn_in=v_w_ffn_in, w_ffn_out=v_w_ffn_out)
    cx, cy, cc = _mesh_pos()
    chip = 2 * cx + cy

    chip_arr = chip.reshape(1).astype(jnp.int32)
    pos_arr = jnp.stack([cc, chip]).astype(jnp.int32)
    bufs = {}
    for n in BIG:
        buf = _cast_into_slot(w[n][0], chip_arr, BF16, f"cast_{n}")
        bufs[n] = buf.reshape(N_CHIPS, 2, buf.shape[1] // 2, buf.shape[2])
    small_bufs = [_cast_into_slot(w[n][0], chip_arr, F32, f"slot_{n}") for n in ("conv_w", "b_gate")]
    w_in_buf, conv_w_buf, b_gate_buf = _allgather_inplace([bufs["w_in"]], small_bufs, "allgather_w_in")
    wts = dict(w_in=w_in_buf.reshape(N_CHIPS, -1, w_in_buf.shape[3]),
               conv_w=conv_w_buf.transpose(1, 0, 2).reshape(CONV_WIDTH, -1),
               b_gate=b_gate_buf.transpose(1, 0, 2).reshape(2, 1, -1),
               norm1_w=norm1_w, q_norm_w=q_norm_w, k_norm_w=k_norm_w, conv_b=conv_b, conv_ln_w=conv_ln_w,
               conv_ln_b=conv_ln_b, norm2_w=norm2_w)

    loss, grad_x, g, reduced, by_chip_w_in = _forward_backward(
        x[0], positions.reshape(-1, 1), loss_target[0], wts, [bufs[n] for n in LATE_GATHER], pos_arr)
    grads = {n: b.reshape(-1, b.shape[2]) for n, b in reduced.items()}
    half_w_in = _sum_chips(by_chip_w_in, pos_arr, "grads_chip_sum_w_in")

    small_parts = [loss] + [g[n] for n in SMALL]
    small_shapes = [p.shape for p in small_parts]
    reduced = _unpack_small(_small_allreduce(_pack_small(small_parts), "small_allreduce"), small_shapes)
    loss_total = reduced[0].reshape(())
    for n, r in zip(SMALL, reduced[1:]):
        grads[n] = r
    ch_shard = conv_w.shape[2]
    grads["conv_w"] = lax.dynamic_slice_in_dim(grads["conv_w"], chip * ch_shard, ch_shard, axis=1)
    d_shard = b_gate.shape[2]
    grads["b_gate"] = lax.dynamic_slice_in_dim(grads["b_gate"], chip * d_shard, d_shard, axis=1)

    delta, new_m, new_v = {}, {}, {}
    for n in EARLY_REDUCE:
        rider = _pair_gather_rider([half_w_in]) if n == "w_ffn_in" else None
        res = _adamw(w[n][0], grads[n], m[n][0], v[n][0], f"adamw_{n}", rider=rider)
        if rider is not None:
            res, (shard_w_in,) = res
            grads["w_in"] = shard_w_in.reshape(-1, shard_w_in.shape[2])
        delta[n], new_m[n], new_v[n] = res
    delta["w_in"], new_m["w_in"], new_v["w_in"] = _adamw(w["w_in"][0], grads["w_in"], m["w_in"][0], v["w_in"][0],
                                                          "adamw_w_in")
    flat2 = lambda a: a.reshape(-1, a.shape[-1])
    d_s, m_s, v_s = _adamw_small([flat2(w[n]) for n in SMALL], [flat2(grads[n]) for n in SMALL],
                                 [flat2(m[n]) for n in SMALL], [flat2(v[n]) for n in SMALL], "adamw_small")
    for i, n in enumerate(SMALL):
        delta[n], new_m[n], new_v[n] = d_s[i], m_s[i], v_s[i]

    shaped = lambda d, n: d[n].reshape(w[n].shape)
    return (loss_total, grad_x[None], *[shaped(grads, n) for n in ORDER], *[shaped(delta, n) for n in ORDER],
            *[shaped(new_m, n) for n in ORDER], *[shaped(new_v, n) for n in ORDER])
```

```python
import functools

import numpy as np
import jax
import jax.numpy as jnp
from jax import lax
from jax.experimental import pallas as pl
from jax.experimental.pallas import tpu as pltpu

F32 = jnp.float32
BF16 = jnp.bfloat16
MESH = pl.DeviceIdType.MESH
ANY = pl.BlockSpec(memory_space=pl.ANY)

HEAD_DIM = 64
N_SLOT_HEADS = 8
DILATIONS = (1, 4, 16)
HALF_SPAN = 64
ROPE_THETA = 500000.0
ROT_DIM = 16
CONV_WIDTH = 31
EPS = 1e-6
NEG_INF = -1e30
ADAM_LR, ADAM_B1, ADAM_B2, ADAM_EPS, ADAM_WD, ADAM_STEP = 0.001, 0.9, 0.999, 1e-08, 0.01, 10

LANES = 128
QBLK = 128
KWIN = QBLK + 2 * HALF_SPAN
VMEM_LIMIT = 48 * 1024 * 1024
N_CHIPS = 4
HIGHEST = lax.Precision.HIGHEST


def _params(**kw):
    return pltpu.CompilerParams(vmem_limit_bytes=VMEM_LIMIT, **kw)


class _Rider:
    def __init__(self, operands, out_shapes, aliases, scratch, start, wait):
        self.operands, self.out_shapes, self.aliases = list(operands), list(out_shapes), dict(aliases)
        self.scratch, self.start, self.wait = list(scratch), start, wait


def _pallas(body, *, name, grid, in_specs, out_specs, out_shape, operands, scratch_shapes=(), aliases=None,
            rider=None):
    single = not isinstance(out_specs, (list, tuple))
    out_specs_l = [out_specs] if single else list(out_specs)
    out_shape_l = [out_shape] if single else list(out_shape)
    aliases = dict(aliases or {})
    if rider is None:
        res = pl.pallas_call(
            body, name=name, grid=grid, in_specs=list(in_specs), out_specs=out_specs_l, out_shape=out_shape_l,
            scratch_shapes=list(scratch_shapes), input_output_aliases=aliases, compiler_params=_params(),
        )(*operands)
        return res[0] if single else res
    n_in, n_rin = len(in_specs), len(rider.operands)
    n_out, n_rout = len(out_specs_l), len(rider.out_shapes)
    n_sc = len(scratch_shapes)

    def wrapped(*refs):
        main_in, r_in = refs[:n_in], refs[n_in:n_in + n_rin]
        o0 = n_in + n_rin
        main_out, r_out = refs[o0:o0 + n_out], refs[o0 + n_out:o0 + n_out + n_rout]
        s0 = o0 + n_out + n_rout
        main_sc, r_sc = refs[s0:s0 + n_sc], refs[s0 + n_sc:]
        ids = [pl.program_id(d) for d in range(len(grid))]
        first = functools.reduce(jnp.logical_and, [i == 0 for i in ids])
        last = functools.reduce(jnp.logical_and, [i == n - 1 for i, n in zip(ids, grid)])

        @pl.when(first)
        def _():
            rider.start(r_in, r_out, r_sc)

        body(*main_in, *main_out, *main_sc)

        @pl.when(last)
        def _():
            rider.wait(r_in, r_out, r_sc)

    for src, dst in rider.aliases.items():
        aliases[n_in + src] = n_out + dst
    res = pl.pallas_call(
        wrapped, name=name, grid=grid, in_specs=list(in_specs) + [ANY] * n_rin,
        out_specs=out_specs_l + [ANY] * n_rout, out_shape=out_shape_l + rider.out_shapes,
        scratch_shapes=list(scratch_shapes) + rider.scratch, input_output_aliases=aliases,
        compiler_params=_params(),
    )(*operands, *rider.operands)
    main = res[:n_out]
    return (main[0] if single else main), res[n_out:]


def _matmul(a, b, *, mode, tm, tn, tk, out_dtype, name, b_blocked=False,
            out_blocked=None, residual=None, rider=None):
    a_shape = a.shape
    if mode == "nn":
        m_dim, k_dim = a_shape
        n_dim = b.shape[0] * b.shape[2] if b_blocked else b.shape[1]
        rows, cols, red = m_dim, n_dim, k_dim
    elif mode == "nt":
        m_dim, n_dim = a_shape
        k_dim = b.shape[1] if b_blocked else b.shape[0]
        rows, cols, red = m_dim, k_dim, n_dim
    else:
        m_dim, k_dim = a_shape
        n_dim = b.shape[1]
        rows, cols, red = k_dim, n_dim, m_dim
    assert rows % tm == 0 and cols % tn == 0 and red % tk == 0, (name, rows, cols, red)
    ni, nj, nk = rows // tm, cols // tn, red // tk

    if mode == "nn":
        a_spec = pl.BlockSpec((tm, tk), lambda i, j, k: (i, k))
        if b_blocked:
            per = b.shape[2] // tn
            b_spec = pl.BlockSpec((None, tk, tn), lambda i, j, k: (j // per, k, j % per))
        else:
            b_spec = pl.BlockSpec((tk, tn), lambda i, j, k: (k, j))
        dims = (((1,), (0,)), ((), ()))
    elif mode == "nt":
        a_spec = pl.BlockSpec((tm, tk), lambda i, j, k: (i, k))
        if b_blocked:
            per = b.shape[2] // tk
            b_spec = pl.BlockSpec((None, tn, tk), lambda i, j, k: (k // per, j, k % per))
        else:
            b_spec = pl.BlockSpec((tn, tk), lambda i, j, k: (j, k))
        dims = (((1,), (1,)), ((), ()))
    else:
        a_spec = pl.BlockSpec((tk, tm), lambda i, j, k: (k, i))
        b_spec = pl.BlockSpec((tk, tn), lambda i, j, k: (k, j))
        dims = (((0,), (0,)), ((), ()))

    if out_blocked:
        per_o = (cols // out_blocked) // tn
        out_spec = pl.BlockSpec((None, tm, tn), lambda i, j, k: (j // per_o, i, j % per_o))
        out_shape = jax.ShapeDtypeStruct((out_blocked, rows, cols // out_blocked), out_dtype)
    else:
        out_spec = pl.BlockSpec((tm, tn), lambda i, j, k: (i, j))
        out_shape = jax.ShapeDtypeStruct((rows, cols), out_dtype)

    in_specs = [a_spec, b_spec]
    operands = [a, b]
    if residual is not None:
        in_specs.append(pl.BlockSpec((tm, tn), lambda i, j, k: (i, j)))
        operands.append(residual)
    has_res = residual is not None

    def body(*refs):
        a_ref, b_ref = refs[0], refs[1]
        res_ref = refs[2] if has_res else None
        o_ref = refs[3] if has_res else refs[2]
        prod = lax.dot_general(a_ref[...], b_ref[...], dims, preferred_element_type=F32)

        def finish(val):
            if has_res:
                val = val + res_ref[...]
            o_ref[...] = val.astype(out_dtype)

        if nk == 1:
            finish(prod)
        else:
            acc_ref = refs[-1]
            k = pl.program_id(2)

            @pl.when(k == 0)
            def _():
                acc_ref[...] = prod

            @pl.when(k > 0)
            def _():
                acc_ref[...] += prod

            @pl.when(k == nk - 1)
            def _():
                finish(acc_ref[...])

    scratch = [pltpu.VMEM((tm, tn), F32)] if nk > 1 else []
    return _pallas(body, name=name, grid=(ni, nj, nk), in_specs=in_specs, out_specs=out_spec,
                   out_shape=out_shape, operands=operands, scratch_shapes=scratch, rider=rider)


def _rmsnorm_fwd(x, w, name):
    s, d = x.shape
    tm = 256

    def body(x_ref, w_ref, o_ref):
        xv = x_ref[...]
        rstd = lax.rsqrt(jnp.mean(xv * xv, axis=-1, keepdims=True) + EPS)
        o_ref[...] = (xv * rstd * w_ref[...]).astype(BF16)

    return pl.pallas_call(
        body, name=name, grid=(s // tm,),
        in_specs=[pl.BlockSpec((tm, d), lambda i: (i, 0)), pl.BlockSpec((1, d), lambda i: (0, 0))],
        out_specs=pl.BlockSpec((tm, d), lambda i: (i, 0)),
        out_shape=jax.ShapeDtypeStruct((s, d), BF16), compiler_params=_params(),
    )(x, w)


def _rmsnorm_bwd(dh, x, w, dres, name):
    s, d = x.shape
    tm = 256

    def body(dh_ref, x_ref, w_ref, dres_ref, dx_ref, dxb_ref, dw_ref):
        xv = x_ref[...]
        rstd = lax.rsqrt(jnp.mean(xv * xv, axis=-1, keepdims=True) + EPS)
        xhat = xv * rstd
        dhv = dh_ref[...]
        g = dhv * w_ref[...]
        dx = rstd * (g - xhat * jnp.mean(g * xhat, axis=-1, keepdims=True)) + dres_ref[...]
        dx_ref[...] = dx
        dxb_ref[...] = dx.astype(BF16)
        part = jnp.sum(dhv * xhat, axis=0, keepdims=True)

        @pl.when(pl.program_id(0) == 0)
        def _():
            dw_ref[...] = part

        @pl.when(pl.program_id(0) > 0)
        def _():
            dw_ref[...] += part

    row = pl.BlockSpec((tm, d), lambda i: (i, 0))
    vec = pl.BlockSpec((1, d), lambda i: (0, 0))
    return pl.pallas_call(
        body, name=name, grid=(s // tm,), in_specs=[row, row, vec, row], out_specs=[row, row, vec],
        out_shape=[jax.ShapeDtypeStruct((s, d), F32), jax.ShapeDtypeStruct((s, d), BF16),
                   jax.ShapeDtypeStruct((1, d), F32)],
        compiler_params=_params(),
    )(dh, x, w, dres)


def _rope_consts():
    lane = np.arange(LANES)
    in_head = lane % HEAD_DIM
    inv_freq = ROPE_THETA ** (-jnp.arange(0, ROT_DIM, 2, dtype=F32) / ROT_DIM)
    invf = jnp.where(jnp.asarray(in_head < ROT_DIM), jnp.tile(inv_freq, LANES // (ROT_DIM // 2)), 0.0)
    m_a = np.where(in_head < ROT_DIM // 2, -1.0, 0.0).astype(np.float32)
    m_b = np.where((in_head >= ROT_DIM // 2) & (in_head < ROT_DIM), 1.0, 0.0).astype(np.float32)
    block_diag = (lane[:, None] // HEAD_DIM == lane[None, :] // HEAD_DIM).astype(np.float32)
    return (invf.reshape(1, LANES).astype(F32), jnp.asarray(m_a).reshape(1, LANES),
            jnp.asarray(m_b).reshape(1, LANES), jnp.asarray(block_diag))


def _head_sums(v, bd):
    return jnp.dot(v, bd, precision=HIGHEST, preferred_element_type=F32)


def _qk_fwd(proj, pos_col, qw2, kw2, consts, name, rider=None):
    s = proj.shape[0]
    width = 3 * N_SLOT_HEADS * HEAD_DIM
    tm = 256
    invf, m_a, m_b, bd = consts
    scale = HEAD_DIM ** -0.5

    def body(q_ref, k_ref, pos_ref, qw_ref, kw_ref, invf_ref, ma_ref, mb_ref, bd_ref, qo_ref, ko_ref):
        ang = pos_ref[...].astype(F32) * invf_ref[...]
        cos = jnp.cos(ang)
        sin = jnp.sin(ang)
        s_a = sin * ma_ref[...]
        s_b = sin * mb_ref[...]
        bdv = bd_ref[...]
        for src, w_ref, dst, sc in ((q_ref, qw_ref, qo_ref, scale), (k_ref, kw_ref, ko_ref, 1.0)):
            for cb in range(width // LANES):
                cols = slice(cb * LANES, (cb + 1) * LANES)
                t = src[:, cols]
                rstd = lax.rsqrt(_head_sums(t * t, bdv) * (1.0 / HEAD_DIM) + EPS)
                y = t * rstd * w_ref[...]
                r = y * cos + pltpu.roll(y, LANES - 8, axis=1) * s_a + pltpu.roll(y, 8, axis=1) * s_b
                dst[:, cols] = r * sc if sc != 1.0 else r

    vec = pl.BlockSpec((1, LANES), lambda i: (0, 0))
    return _pallas(
        body, name=name, grid=(s // tm,),
        in_specs=[pl.BlockSpec((tm, width), lambda i: (i, 0)), pl.BlockSpec((tm, width), lambda i: (i, 1)),
                  pl.BlockSpec((tm, 1), lambda i: (i, 0)), vec, vec, vec, vec, vec,
                  pl.BlockSpec((LANES, LANES), lambda i: (0, 0))],
        out_specs=[pl.BlockSpec((tm, width), lambda i: (i, 0))] * 2,
        out_shape=[jax.ShapeDtypeStruct((s, width), F32)] * 2,
        operands=[proj, proj, pos_col, qw2, kw2, invf, m_a, m_b, bd], rider=rider)


def _qk_bwd(dproj, dqn, dkn, dv, da, db, proj, pos_col, qw2, kw2, consts, name, rider=None):
    s = proj.shape[0]
    width = 3 * N_SLOT_HEADS * HEAD_DIM
    ch = da.shape[1]
    out_w = 3 * width + 2 * ch
    tm = 256
    invf, m_a, m_b, bd = consts
    scale = HEAD_DIM ** -0.5

    def body(dproj_in, dq_ref, dk_ref, dv_ref, da_ref, db_ref, q_ref, k_ref, pos_ref, qw_ref, kw_ref,
             invf_ref, ma_ref, mb_ref, bd_ref, out_ref, dqw_ref, dkw_ref):
        del dproj_in
        ang = pos_ref[...].astype(F32) * invf_ref[...]
        cos = jnp.cos(ang)
        sin = jnp.sin(ang)
        s_a = sin * ma_ref[...]
        s_b = sin * mb_ref[...]
        bdv = bd_ref[...]
        first = pl.program_id(0) == 0
        for src, dsrc, w_ref, col0, dw_ref, sc in ((q_ref, dq_ref, qw_ref, 0, dqw_ref, scale),
                                                   (k_ref, dk_ref, kw_ref, width, dkw_ref, 1.0)):
            dw_acc = jnp.zeros((1, LANES), F32)
            for cb in range(width // LANES):
                cols = slice(cb * LANES, (cb + 1) * LANES)
                t = src[:, cols]
                dr = dsrc[:, cols]
                if sc != 1.0:
                    dr = dr * sc
                dy = dr * cos + pltpu.roll(dr * s_a, 8, axis=1) + pltpu.roll(dr * s_b, LANES - 8, axis=1)
                rstd = lax.rsqrt(_head_sums(t * t, bdv) * (1.0 / HEAD_DIM) + EPS)
                xhat = t * rstd
                g = dy * w_ref[...]
                dt = rstd * (g - xhat * (_head_sums(g * xhat, bdv) * (1.0 / HEAD_DIM)))
                out_ref[:, col0 + cb * LANES: col0 + (cb + 1) * LANES] = dt.astype(BF16)
                dw_acc = dw_acc + jnp.sum(dy * xhat, axis=0, keepdims=True)
            dw_acc = dw_acc + pltpu.roll(dw_acc, HEAD_DIM, axis=1)

            @pl.when(first)
            def _(dw_ref=dw_ref, dw_acc=dw_acc):
                dw_ref[...] = dw_acc

            @pl.when(jnp.logical_not(first))
            def _(dw_ref=dw_ref, dw_acc=dw_acc):
                dw_ref[...] += dw_acc
        out_ref[:, 2 * width: 3 * width] = dv_ref[...].astype(BF16)
        out_ref[:, 3 * width: 3 * width + ch] = da_ref[...]
        out_ref[:, 3 * width + ch: out_w] = db_ref[...]

    vec = pl.BlockSpec((1, LANES), lambda i: (0, 0))
    blk = lambda c: pl.BlockSpec((tm, width), lambda i: (i, c))
    cblk = pl.BlockSpec((tm, ch), lambda i: (i, 0))
    return _pallas(
        body, name=name, grid=(s // tm,),
        in_specs=[ANY, blk(0), blk(0), blk(0), cblk, cblk, blk(0), blk(1),
                  pl.BlockSpec((tm, 1), lambda i: (i, 0)), vec, vec, vec, vec, vec,
                  pl.BlockSpec((LANES, LANES), lambda i: (0, 0))],
        out_specs=[pl.BlockSpec((tm, out_w), lambda i: (i, 0)), vec, vec],
        out_shape=[jax.ShapeDtypeStruct(dproj.shape, BF16)] + [jax.ShapeDtypeStruct((1, LANES), F32)] * 2,
        operands=[dproj, dqn, dkn, dv, da, db, proj, proj, pos_col, qw2, kw2, invf, m_a, m_b, bd],
        aliases={0: 0}, rider=rider)


def _row_chunks(n_rows, fn, chunk=256):
    def step(i, c):
        fn(pl.ds(pl.multiple_of(i * chunk, chunk), chunk))
        return c
    lax.fori_loop(0, n_rows // chunk, step, 0)


def _to_residue_major(dst, src, s, d, dst_off=0, cast=None):
    seq = s // d
    for r in range(d):
        v = src[...] if d == 1 else src[pl.ds(r, seq, stride=d), :]
        dst[dst_off + r * seq: dst_off + (r + 1) * seq, :] = v if cast is None else v.astype(cast)


def _from_residue_major(dst, src, s, d, src_off=0):
    seq = s // d
    for r in range(d):
        v = src[src_off + r * seq: src_off + (r + 1) * seq, :]
        if d == 1:
            dst[...] = v
        else:
            dst[pl.ds(r, seq, stride=d), :] = v


def _band_bias():
    qi = lax.broadcasted_iota(jnp.int32, (QBLK, KWIN), 0)
    kj = lax.broadcasted_iota(jnp.int32, (QBLK, KWIN), 1)
    return jnp.where(jnp.abs(kj - HALF_SPAN - qi) <= HALF_SPAN, 0.0, NEG_INF).astype(F32)


def _range_bias(base, seq):
    kj = lax.broadcasted_iota(jnp.int32, (1, KWIN), 1)
    lo = (base & -seq) - base + HALF_SPAN
    return jnp.where((kj >= lo) & (kj < lo + seq), 0.0, NEG_INF).astype(F32)


def _block_base(b):
    return b * QBLK if isinstance(b, int) else pl.multiple_of(b * QBLK, QBLK)


def _attn_fwd(qn, kn, proj, name, rider=None):
    s = qn.shape[0]
    n_pairs = N_SLOT_HEADS * HEAD_DIM // LANES
    v_col0 = 2 * qn.shape[1] // LANES
    nt_dims = (((1,), (1,)), ((), ()))

    def body(q_ref, k_ref, v_ref, attn_ref, lse_ref, q_rm, k_rm, v_rm, acc_rm, m_rm, l_rm,
             acc_p, m_p, l_p, m_run, l_run, acc_run, band, s_buf, m_buf):
        g = pl.program_id(1)
        zpad = jnp.zeros((HALF_SPAN, LANES), BF16)
        k_rm[0:HALF_SPAN, :] = zpad
        k_rm[s + HALF_SPAN: s + 2 * HALF_SPAN, :] = zpad
        v_rm[0:HALF_SPAN, 0:LANES] = zpad
        v_rm[s + HALF_SPAN: s + 2 * HALF_SPAN, 0:LANES] = zpad

        def ones_rows(rows):
            v_rm[pl.ds(rows.start, rows.size), LANES:2 * LANES] = jnp.ones((rows.size, LANES), BF16)

        _row_chunks(s + 2 * HALF_SPAN, ones_rows, chunk=2 * HALF_SPAN)
        band[...] = _band_bias()
        lane = lax.broadcasted_iota(jnp.int32, (QBLK, LANES), 1)
        low = lane < HEAD_DIM
        n_blk = s // QBLK

        for gi, d in enumerate(DILATIONS):
            @pl.when(g == gi)
            def _(gi=gi, d=d):
                seq = s // d
                _to_residue_major(q_rm, q_ref, s, d, cast=BF16)
                _to_residue_major(k_rm, k_ref, s, d, dst_off=HALF_SPAN, cast=BF16)
                _to_residue_major(v_rm.at[:, 0:LANES], v_ref, s, d, dst_off=HALF_SPAN, cast=BF16)

                def scores(b, slot):
                    base = _block_base(b)
                    q = q_rm[pl.ds(base, QBLK), :]
                    zero = jnp.zeros_like(q)
                    q2 = jnp.concatenate([jnp.where(low, q, zero), jnp.where(low, zero, q)], axis=0)
                    sc = lax.dot_general(q2, k_rm[pl.ds(base, KWIN), :], nt_dims, preferred_element_type=F32)
                    bias = band[...] + _range_bias(base, seq)
                    for hh in range(2):
                        rows = slice(hh * QBLK, (hh + 1) * QBLK)
                        sh = sc[rows, :] + bias
                        s_buf[slot, rows, :] = sh
                        m_buf[slot, rows, :] = jnp.broadcast_to(jnp.max(sh, axis=-1, keepdims=True), (QBLK, LANES))

                def outputs(b, slot):
                    base = _block_base(b)
                    sv = s_buf[slot]
                    mb = m_buf[slot]
                    p = jnp.exp(jnp.concatenate([sv[:, 0:LANES] - mb, sv[:, LANES:2 * LANES] - mb], axis=1))
                    pv = jnp.dot(p.astype(BF16), v_rm[pl.ds(base, KWIN), :], preferred_element_type=F32)
                    rows = pl.ds(base, QBLK)
                    acc_rm[rows, :] = jnp.where(low, pv[0:QBLK, 0:LANES], pv[QBLK:2 * QBLK, 0:LANES])
                    l_rm[rows, :] = jnp.where(low, pv[0:QBLK, LANES:2 * LANES], pv[QBLK:2 * QBLK, LANES:2 * LANES])
                    m_rm[rows, :] = jnp.where(low, mb[0:QBLK, :], mb[QBLK:2 * QBLK, :])

                scores(0, 0)

                def pair(i, carry):
                    b = 2 * i
                    outputs(b, 0)
                    scores(b + 1, 1)
                    outputs(b + 1, 1)
                    scores(b + 2, 0)
                    return carry

                lax.fori_loop(0, n_blk // 2 - 1, pair, 0)
                outputs(n_blk - 2, 0)
                scores(n_blk - 1, 1)
                outputs(n_blk - 1, 1)
                if d == 1:
                    src = (acc_rm, m_rm, l_rm)
                else:
                    for dst_, src_ in ((acc_p, acc_rm), (m_p, m_rm), (l_p, l_rm)):
                        _from_residue_major(dst_, src_, s, d)
                    src = (acc_p, m_p, l_p)

                def combine(rows):
                    a_g, m_g, l_g = src[0][rows, :], src[1][rows, :], src[2][rows, :]
                    if gi == 0:
                        m_new, l_new, a_new = m_g, l_g, a_g
                    else:
                        m_old = m_run[rows, :]
                        m_new = jnp.maximum(m_old, m_g)
                        w_old = jnp.exp(m_old - m_new)
                        w_g = jnp.exp(m_g - m_new)
                        l_new = l_run[rows, :] * w_old + l_g * w_g
                        a_new = acc_run[rows, :] * w_old + a_g * w_g
                    if gi == len(DILATIONS) - 1:
                        attn_ref[rows, :] = a_new / l_new
                        lse_ref[rows, :] = m_new + jnp.log(l_new)
                    else:
                        m_run[rows, :] = m_new
                        l_run[rows, :] = l_new
                        acc_run[rows, :] = a_new

                _row_chunks(s, combine)

    qk_spec = pl.BlockSpec((s, LANES), lambda hp, g: (0, g * n_pairs + hp))
    v_spec = pl.BlockSpec((s, LANES), lambda hp, g: (0, v_col0 + g * n_pairs + hp))
    o_spec = pl.BlockSpec((s, LANES), lambda hp, g: (0, hp))
    f32buf = pltpu.VMEM((s, LANES), F32)
    return _pallas(
        body, name=name, grid=(n_pairs, len(DILATIONS)), in_specs=[qk_spec, qk_spec, v_spec],
        out_specs=[o_spec, o_spec],
        out_shape=[jax.ShapeDtypeStruct((s, n_pairs * LANES), F32)] * 2,
        operands=[qn, kn, proj],
        scratch_shapes=[pltpu.VMEM((s, LANES), BF16), pltpu.VMEM((s + 2 * HALF_SPAN, LANES), BF16),
                        pltpu.VMEM((s + 2 * HALF_SPAN, 2 * LANES), BF16)] + [f32buf] * 9
        + [pltpu.VMEM((QBLK, KWIN), F32), pltpu.VMEM((2, 2 * QBLK, KWIN), F32),
           pltpu.VMEM((2, 2 * QBLK, LANES), F32)],
        rider=rider)


def _attn_bwd(qn, kn, proj, dattn, attn, lse, bd, name, rider=None):
    s = qn.shape[0]
    n_pairs = N_SLOT_HEADS * HEAD_DIM // LANES
    v_col0 = 2 * qn.shape[1] // LANES
    nt_dims = (((1,), (1,)), ((), ()))
    tn_dims = (((0,), (0,)), ((), ()))
    spad = s + 2 * HALF_SPAN

    def body(q_ref, k_ref, v_ref, do_ref, o_ref, lse_ref, bd_ref, dq_ref, dk_ref, dv_ref,
             q_rm, k_rm, v_rm, do_rm, lse0_rm, lse1_rm, dd0_rm, dd1_rm, dq_rm, dk_rm, dv_rm,
             lse0_p, lse1_p, dd0_p, dd1_p, band, p_buf, ds_buf):
        g = pl.program_id(1)
        zpad = jnp.zeros((HALF_SPAN, LANES), BF16)
        for buf in (k_rm, v_rm):
            buf[0:HALF_SPAN, :] = zpad
            buf[s + HALF_SPAN: spad, :] = zpad
        zf = jnp.zeros((HALF_SPAN, LANES), F32)
        for buf in (dk_rm, dv_rm):
            buf[0:HALF_SPAN, :] = zf
            buf[s + HALF_SPAN: spad, :] = zf
        band[...] = _band_bias()

        def clear(rows):
            z = jnp.zeros((rows.size, LANES), F32)
            dk_rm[pl.ds(rows.start + HALF_SPAN, rows.size), :] = z
            dv_rm[pl.ds(rows.start + HALF_SPAN, rows.size), :] = z

        _row_chunks(s, clear)

        def prepare(rows):
            lo = lax.broadcasted_iota(jnp.int32, (rows.size, LANES), 1) < HEAD_DIM
            dsum = _head_sums(do_ref[rows, :] * o_ref[rows, :], bd_ref[...])
            dswap = pltpu.roll(dsum, HEAD_DIM, axis=1)
            dd0_p[rows, :] = jnp.where(lo, dsum, dswap)
            dd1_p[rows, :] = jnp.where(lo, dswap, dsum)
            lv = lse_ref[rows, :]
            lswap = pltpu.roll(lv, HEAD_DIM, axis=1)
            lse0_p[rows, :] = jnp.where(lo, lv, lswap)
            lse1_p[rows, :] = jnp.where(lo, lswap, lv)

        @pl.when(g == 0)
        def _():
            _row_chunks(s, prepare)
        lane = lax.broadcasted_iota(jnp.int32, (QBLK, LANES), 1)
        low = lane < HEAD_DIM
        n_blk = s // QBLK

        def stacked(ref, rows):
            val = ref[rows, :]
            zero = jnp.zeros_like(val)
            return jnp.concatenate([jnp.where(low, val, zero), jnp.where(low, zero, val)], axis=0)

        for gi, d in enumerate(DILATIONS):
            @pl.when(g == gi)
            def _(d=d):
                seq = s // d
                _to_residue_major(q_rm, q_ref, s, d, cast=BF16)
                _to_residue_major(k_rm, k_ref, s, d, dst_off=HALF_SPAN, cast=BF16)
                _to_residue_major(v_rm, v_ref, s, d, dst_off=HALF_SPAN, cast=BF16)
                _to_residue_major(do_rm, do_ref, s, d, cast=BF16)
                for dst_, src_ in ((lse0_rm, lse0_p), (lse1_rm, lse1_p), (dd0_rm, dd0_p), (dd1_rm, dd1_p)):
                    _to_residue_major(dst_, src_, s, d)

                def scores(b, slot):
                    base = _block_base(b)
                    rows = pl.ds(base, QBLK)
                    win = pl.ds(base, KWIN)
                    sc = lax.dot_general(stacked(q_rm, rows), k_rm[win, :], nt_dims, preferred_element_type=F32)
                    dp = lax.dot_general(stacked(do_rm, rows), v_rm[win, :], nt_dims, preferred_element_type=F32)
                    bias = band[...] + _range_bias(base, seq)
                    for hh, (lse_r, dd_r) in enumerate(((lse0_rm, dd0_rm), (lse1_rm, dd1_rm))):
                        r = slice(hh * QBLK, (hh + 1) * QBLK)
                        lse_h = lse_r[rows, :]
                        dd_h = dd_r[rows, :]
                        sh = sc[r, :] + bias
                        p = jnp.exp(jnp.concatenate([sh[:, 0:LANES] - lse_h, sh[:, LANES:KWIN] - lse_h], axis=1))
                        dph = dp[r, :]
                        ds = p * jnp.concatenate([dph[:, 0:LANES] - dd_h, dph[:, LANES:KWIN] - dd_h], axis=1)
                        p_buf[slot, r, :] = p.astype(BF16)
                        ds_buf[slot, r, :] = ds.astype(BF16)

                def grads(b, slot):
                    base = _block_base(b)
                    rows = pl.ds(base, QBLK)
                    win = pl.ds(base, KWIN)
                    p = p_buf[slot]
                    ds = ds_buf[slot]
                    dq2 = jnp.dot(ds, k_rm[win, :], preferred_element_type=F32)
                    dq_rm[rows, :] = jnp.where(low, dq2[0:QBLK, :], dq2[QBLK:2 * QBLK, :])
                    dk_rm[win, :] += lax.dot_general(ds, stacked(q_rm, rows), tn_dims, preferred_element_type=F32)
                    dv_rm[win, :] += lax.dot_general(p, stacked(do_rm, rows), tn_dims, preferred_element_type=F32)

                scores(0, 0)

                def pair(i, carry):
                    b = 2 * i
                    grads(b, 0)
                    scores(b + 1, 1)
                    grads(b + 1, 1)
                    scores(b + 2, 0)
                    return carry

                lax.fori_loop(0, n_blk // 2 - 1, pair, 0)
                grads(n_blk - 2, 0)
                scores(n_blk - 1, 1)
                grads(n_blk - 1, 1)
                _from_residue_major(dq_ref, dq_rm, s, d)
                _from_residue_major(dk_ref, dk_rm, s, d, src_off=HALF_SPAN)
                _from_residue_major(dv_ref, dv_rm, s, d, src_off=HALF_SPAN)

    qk_spec = pl.BlockSpec((s, LANES), lambda hp, g: (0, g * n_pairs + hp))
    v_spec = pl.BlockSpec((s, LANES), lambda hp, g: (0, v_col0 + g * n_pairs + hp))
    o_spec = pl.BlockSpec((s, LANES), lambda hp, g: (0, hp))
    width = qn.shape[1]
    f32buf = pltpu.VMEM((s, LANES), F32)
    f32pad = pltpu.VMEM((spad, LANES), F32)
    return _pallas(
        body, name=name, grid=(n_pairs, len(DILATIONS)),
        in_specs=[qk_spec, qk_spec, v_spec, o_spec, o_spec, o_spec,
                  pl.BlockSpec((LANES, LANES), lambda hp, g: (0, 0))],
        out_specs=[qk_spec, qk_spec, qk_spec],
        out_shape=[jax.ShapeDtypeStruct((s, width), F32)] * 3,
        operands=[qn, kn, proj, dattn, attn, lse, bd],
        scratch_shapes=[pltpu.VMEM((s, LANES), BF16), pltpu.VMEM((spad, LANES), BF16),
                        pltpu.VMEM((spad, LANES), BF16), pltpu.VMEM((s, LANES), BF16),
                        f32buf, f32buf, f32buf, f32buf, f32buf, f32pad, f32pad,
                        f32buf, f32buf, f32buf, f32buf, pltpu.VMEM((QBLK, KWIN), F32),
                        pltpu.VMEM((2, 2 * QBLK, KWIN), BF16), pltpu.VMEM((2, 2 * QBLK, KWIN), BF16)],
        rider=rider)


CONV_PAD = 16


def _conv_fwd(proj, conv_w, conv_b, col0, name, rider=None):
    s = proj.shape[0]
    ch = conv_w.shape[1]
    nblk = ch // LANES
    a0 = col0 // LANES
    tr = 256
    shift = CONV_PAD - (CONV_WIDTH - 1) // 2

    def body(a_ref, b_ref, w_ref, bias_ref, u0_ref, uc_ref, pad):
        z = jnp.zeros((CONV_PAD, LANES), F32)
        pad[0:CONV_PAD, :] = z
        pad[s + CONV_PAD: s + 2 * CONV_PAD, :] = z

        def glu(rows):
            u0 = a_ref[rows, :] * jax.nn.sigmoid(b_ref[rows, :])
            u0_ref[rows, :] = u0
            pad[pl.ds(rows.start + CONV_PAD, rows.size), :] = u0

        _row_chunks(s, glu)
        for t in range(0, s, tr):
            acc = jnp.broadcast_to(bias_ref[...], (tr, LANES))
            for k in range(CONV_WIDTH):
                acc = acc + w_ref[k:k + 1, :] * pad[t + k + shift: t + k + shift + tr, :]
            uc_ref[t:t + tr, :] = acc

    return _pallas(
        body, name=name, grid=(nblk,),
        in_specs=[pl.BlockSpec((s, LANES), lambda c: (0, a0 + c)),
                  pl.BlockSpec((s, LANES), lambda c: (0, a0 + nblk + c)),
                  pl.BlockSpec((CONV_WIDTH, LANES), lambda c: (0, c)),
                  pl.BlockSpec((1, LANES), lambda c: (0, c))],
        out_specs=[pl.BlockSpec((s, LANES), lambda c: (0, c))] * 2,
        out_shape=[jax.ShapeDtypeStruct((s, ch), F32)] * 2, operands=[proj, proj, conv_w, conv_b],
        scratch_shapes=[pltpu.VMEM((s + 2 * CONV_PAD, LANES), F32)], rider=rider)


def _ln_silu_fwd(uc, ln_w, ln_b, name):
    s, ch = uc.shape
    tm = 256

    def body(u_ref, w_ref, b_ref, o_ref):
        u = u_ref[...]
        mu = jnp.mean(u, axis=-1, keepdims=True)
        xc = u - mu
        rstd = lax.rsqrt(jnp.mean(xc * xc, axis=-1, keepdims=True) + EPS)
        z = xc * rstd * w_ref[...] + b_ref[...]
        o_ref[...] = (z * jax.nn.sigmoid(z)).astype(BF16)

    row = pl.BlockSpec((tm, ch), lambda i: (i, 0))
    vec = pl.BlockSpec((1, ch), lambda i: (0, 0))
    return pl.pallas_call(
        body, name=name, grid=(s // tm,), in_specs=[row, vec, vec], out_specs=row,
        out_shape=jax.ShapeDtypeStruct((s, ch), BF16), compiler_params=_params(),
    )(uc, ln_w, ln_b)


def _ln_silu_bwd(du3, uc, ln_w, ln_b, name):
    s, ch = uc.shape
    tm = 256

    def body(d_ref, u_ref, w_ref, b_ref, du_ref, dw_ref, db_ref):
        u = u_ref[...]
        mu = jnp.mean(u, axis=-1, keepdims=True)
        xc = u - mu
        rstd = lax.rsqrt(jnp.mean(xc * xc, axis=-1, keepdims=True) + EPS)
        xhat = xc * rstd
        z = xhat * w_ref[...] + b_ref[...]
        sg = jax.nn.sigmoid(z)
        dz = d_ref[...] * (sg * (1.0 + z * (1.0 - sg)))
        dxh = dz * w_ref[...]
        du_ref[...] = rstd * (dxh - jnp.mean(dxh, axis=-1, keepdims=True)
                              - xhat * jnp.mean(dxh * xhat, axis=-1, keepdims=True))
        pw = jnp.sum(dz * xhat, axis=0, keepdims=True)
        pb = jnp.sum(dz, axis=0, keepdims=True)
        first = pl.program_id(0) == 0

        @pl.when(first)
        def _():
            dw_ref[...] = pw
            db_ref[...] = pb

        @pl.when(jnp.logical_not(first))
        def _():
            dw_ref[...] += pw
            db_ref[...] += pb

    row = pl.BlockSpec((tm, ch), lambda i: (i, 0))
    vec = pl.BlockSpec((1, ch), lambda i: (0, 0))
    return pl.pallas_call(
        body, name=name, grid=(s // tm,), in_specs=[row, row, vec, vec], out_specs=[row, vec, vec],
        out_shape=[jax.ShapeDtypeStruct((s, ch), F32), jax.ShapeDtypeStruct((1, ch), F32),
                   jax.ShapeDtypeStruct((1, ch), F32)],
        compiler_params=_params(),
    )(du3, uc, ln_w, ln_b)


def _conv_bwd(duc, u0, proj, conv_w, col0, name, rider=None):
    s = proj.shape[0]
    ch = conv_w.shape[1]
    nblk = ch // LANES
    a0 = col0 // LANES
    tr = 256
    half = (CONV_WIDTH - 1) // 2
    shift = CONV_PAD - half

    def body(duc_ref, u0_ref, a_ref, b_ref, w_ref, da_ref, db_ref, dw_ref, dbias_ref, pad_d, pad_u):
        z = jnp.zeros((CONV_PAD, LANES), F32)
        for buf in (pad_d, pad_u):
            buf[0:CONV_PAD, :] = z
            buf[s + CONV_PAD: s + 2 * CONV_PAD, :] = z

        def fill(rows):
            dst = pl.ds(rows.start + CONV_PAD, rows.size)
            pad_d[dst, :] = duc_ref[rows, :]
            pad_u[dst, :] = u0_ref[rows, :]

        _row_chunks(s, fill)
        dw_acc = [jnp.zeros((8, LANES), F32) for _ in range(CONV_WIDTH)]
        dbias_acc = jnp.zeros((8, LANES), F32)
        for t in range(0, s, tr):
            d_t = duc_ref[t:t + tr, :]
            dbias_acc = dbias_acc + jnp.sum(d_t.reshape(tr // 8, 8, LANES), axis=0)
            du0 = jnp.zeros((tr, LANES), F32)
            for k in range(CONV_WIDTH):
                du0 = du0 + w_ref[k:k + 1, :] * pad_d[t - k + half + CONV_PAD: t - k + half + CONV_PAD + tr, :]
                prod = d_t * pad_u[t + k + shift: t + k + shift + tr, :]
                dw_acc[k] = dw_acc[k] + jnp.sum(prod.reshape(tr // 8, 8, LANES), axis=0)
            av = a_ref[t:t + tr, :]
            sg = jax.nn.sigmoid(b_ref[t:t + tr, :])
            da_ref[t:t + tr, :] = (du0 * sg).astype(BF16)
            db_ref[t:t + tr, :] = (du0 * av * sg * (1.0 - sg)).astype(BF16)
        for k in range(CONV_WIDTH):
            dw_ref[k:k + 1, :] = jnp.sum(dw_acc[k], axis=0, keepdims=True)
        dbias_ref[...] = jnp.sum(dbias_acc, axis=0, keepdims=True)

    col = lambda off: pl.BlockSpec((s, LANES), lambda c: (0, off + c))
    return _pallas(
        body, name=name, grid=(nblk,),
        in_specs=[col(0), col(0), col(a0), col(a0 + nblk),
                  pl.BlockSpec((CONV_WIDTH, LANES), lambda c: (0, c))],
        out_specs=[col(0), col(0), pl.BlockSpec((CONV_WIDTH, LANES), lambda c: (0, c)),
                   pl.BlockSpec((1, LANES), lambda c: (0, c))],
        out_shape=[jax.ShapeDtypeStruct((s, ch), BF16)] * 2
        + [jax.ShapeDtypeStruct((CONV_WIDTH, ch), F32), jax.ShapeDtypeStruct((1, ch), F32)],
        operands=[duc, u0, proj, proj, conv_w],
        scratch_shapes=[pltpu.VMEM((s + 2 * CONV_PAD, LANES), F32)] * 2, rider=rider)


GATE_BLK = 512


def _gate_fwd(proj, bg, y_a, y_b, col0, name):
    s, d = y_a.shape
    tm = 256
    g0 = col0 // GATE_BLK
    nb = d // GATE_BLK

    def body(ga_ref, gb_ref, ba_ref, bb_ref, ya_ref, yb_ref, o_ref):
        g_a = jax.nn.sigmoid(ga_ref[...] + ba_ref[...])
        g_b = jax.nn.sigmoid(gb_ref[...] + bb_ref[...])
        o_ref[...] = (g_a * ya_ref[...] + g_b * yb_ref[...]).astype(BF16)

    act = pl.BlockSpec((tm, GATE_BLK), lambda i, j: (i, j))
    return pl.pallas_call(
        body, name=name, grid=(s // tm, nb),
        in_specs=[pl.BlockSpec((tm, GATE_BLK), lambda i, j: (i, g0 + j)),
                  pl.BlockSpec((tm, GATE_BLK), lambda i, j: (i, g0 + nb + j)),
                  pl.BlockSpec((None, 1, GATE_BLK), lambda i, j: (0, 0, j)),
                  pl.BlockSpec((None, 1, GATE_BLK), lambda i, j: (1, 0, j)), act, act],
        out_specs=act, out_shape=jax.ShapeDtypeStruct((s, d), BF16), compiler_params=_params(),
    )(proj, proj, bg, bg, y_a, y_b)


def _gate_bwd(d_mixed, proj, bg, y_a, y_b, col0, name, rider=None):
    s, d = y_a.shape
    tm = 256
    g0 = col0 // GATE_BLK
    nb = d // GATE_BLK

    def body(dm_ref, gl_ref, b_ref, y_ref, dgl_ref, dy_ref, db_ref):
        gate = jax.nn.sigmoid(gl_ref[...] + b_ref[...])
        dm = dm_ref[...]
        dy_ref[...] = (dm * gate).astype(BF16)
        dgl = dm * y_ref[...] * gate * (1.0 - gate)
        dgl_ref[...] = dgl.astype(BF16)
        part = jnp.sum(dgl, axis=0, keepdims=True)
        first = pl.program_id(1) == 0

        @pl.when(first)
        def _():
            db_ref[...] = part

        @pl.when(jnp.logical_not(first))
        def _():
            db_ref[...] += part

    def call(y, branch, dproj_prev, rider):
        def wrapped(*refs):
            if dproj_prev is not None:
                refs = refs[1:]
            body(*refs)

        ins = [pl.BlockSpec((tm, GATE_BLK), lambda j, i: (i, j)),
               pl.BlockSpec((tm, GATE_BLK), lambda j, i: (i, g0 + branch * nb + j)),
               pl.BlockSpec((None, 1, GATE_BLK), lambda j, i: (branch, 0, j)),
               pl.BlockSpec((tm, GATE_BLK), lambda j, i: (i, j))]
        ops = [d_mixed, proj, bg, y]
        alias = {}
        if dproj_prev is not None:
            ins = [ANY] + ins
            ops = [dproj_prev] + ops
            alias = {0: 0}
        return _pallas(
            wrapped, name=f"{name}_{branch}", grid=(nb, s // tm), in_specs=ins,
            out_specs=[pl.BlockSpec((tm, GATE_BLK), lambda j, i: (i, g0 + branch * nb + j)),
                       pl.BlockSpec((tm, GATE_BLK), lambda j, i: (i, j)),
                       pl.BlockSpec((1, GATE_BLK), lambda j, i: (0, j))],
            out_shape=[jax.ShapeDtypeStruct((s, proj.shape[1]), BF16), jax.ShapeDtypeStruct((s, d), BF16),
                       jax.ShapeDtypeStruct((1, d), F32)],
            operands=ops, aliases=alias, rider=rider)

    first = call(y_a, 0, None, rider)
    (dproj, dy_a, db_a), rider_out = first if rider is not None else (first, None)
    dproj, dy_b, db_b = call(y_b, 1, dproj, None)
    res = (dproj, dy_a, dy_b, db_a, db_b)
    return res if rider is None else (res, rider_out)


def _ffn_in_swiglu(h2, w_blocked, name):
    s, k = h2.shape
    nblk, _, tn = w_blocked.shape
    ff = nblk // 2 * tn
    tm = 512

    def body(a_ref, wg_ref, wu_ref, g_ref, u_ref, act_ref):
        a = a_ref[...]
        gt = jnp.dot(a, wg_ref[...], preferred_element_type=F32)
        up = jnp.dot(a, wu_ref[...], preferred_element_type=F32)
        g_ref[...] = gt
        u_ref[...] = up
        act_ref[...] = (gt * jax.nn.sigmoid(gt) * up).astype(BF16)

    out = pl.BlockSpec((tm, tn), lambda i, j: (i, j))
    return pl.pallas_call(
        body, name=name, grid=(s // tm, nblk // 2),
        in_specs=[pl.BlockSpec((tm, k), lambda i, j: (i, 0)),
                  pl.BlockSpec((None, k, tn), lambda i, j: (j, 0, 0)),
                  pl.BlockSpec((None, k, tn), lambda i, j: (nblk // 2 + j, 0, 0))],
        out_specs=[out, out, out],
        out_shape=[jax.ShapeDtypeStruct((s, ff), F32), jax.ShapeDtypeStruct((s, ff), F32),
                   jax.ShapeDtypeStruct((s, ff), BF16)],
        compiler_params=_params(),
    )(h2, w_blocked, w_blocked)


def _swiglu_bwd(gate, up, d_act, name, rider=None):
    s, ff = gate.shape
    tm = 256

    def body(g_ref, u_ref, d_ref, o_ref):
        gt = g_ref[...]
        sg = jax.nn.sigmoid(gt)
        dv = d_ref[...]
        o_ref[:, 0:ff] = (dv * u_ref[...] * (sg * (1.0 + gt * (1.0 - sg)))).astype(BF16)
        o_ref[:, ff:2 * ff] = (dv * gt * sg).astype(BF16)

    row = pl.BlockSpec((tm, ff), lambda i: (i, 0))
    return _pallas(
        body, name=name, grid=(s // tm,), in_specs=[row, row, row],
        out_specs=pl.BlockSpec((tm, 2 * ff), lambda i: (i, 0)),
        out_shape=jax.ShapeDtypeStruct((s, 2 * ff), BF16), operands=[gate, up, d_act], rider=rider)


def _loss_fwd_bwd(y, target, name):
    s, d = y.shape
    tm = 256

    def body(y_ref, t_ref, dy_ref, dyb_ref, loss_ref, acc):
        diff = y_ref[...] - t_ref[...]
        dy = diff * (1.0 / d)
        dy_ref[...] = dy
        dyb_ref[...] = dy.astype(BF16)
        part = jnp.sum((diff * diff).reshape(tm // 8, 8, d), axis=0)
        i = pl.program_id(0)

        @pl.when(i == 0)
        def _():
            acc[...] = part

        @pl.when(i > 0)
        def _():
            acc[...] += part

        @pl.when(i == pl.num_programs(0) - 1)
        def _():
            loss_ref[...] = (0.5 / d) * jnp.sum(jnp.sum(acc[...], axis=1, keepdims=True), axis=0, keepdims=True)

    row = pl.BlockSpec((tm, d), lambda i: (i, 0))
    return pl.pallas_call(
        body, name=name, grid=(s // tm,), in_specs=[row, row],
        out_specs=[row, row, pl.BlockSpec((1, 1), lambda i: (0, 0))],
        out_shape=[jax.ShapeDtypeStruct((s, d), F32), jax.ShapeDtypeStruct((s, d), BF16),
                   jax.ShapeDtypeStruct((1, 1), F32)],
        scratch_shapes=[pltpu.VMEM((8, d), F32)], compiler_params=_params(),
    )(y, target)


LATE_GATHER = ("w_o_attn", "w_pw_conv", "w_out", "w_ffn_in", "w_ffn_out")
EARLY_REDUCE = LATE_GATHER


def _blocks_by_half(g):
    if g.ndim == 2:
        g = g.reshape(N_CHIPS, g.shape[0] // N_CHIPS, g.shape[1])
    return g.reshape(N_CHIPS, 2, g.shape[1] // 2, g.shape[2])


def _forward_backward(x, pos_col, target, wts, late_bufs, pos_arr):
    wts = dict(wts)
    consts = _rope_consts()
    bd = consts[3]
    qw2 = jnp.tile(wts["q_norm_w"], (1, LANES // HEAD_DIM))
    kw2 = jnp.tile(wts["k_norm_w"], (1, LANES // HEAD_DIM))
    qkv_w = 3 * N_SLOT_HEADS * HEAD_DIM
    conv_col0 = 3 * qkv_w
    ch = wts["conv_w"].shape[1]
    gate_col0 = conv_col0 + 2 * ch

    h = _rmsnorm_fwd(x, wts["norm1_w"], "rms1_fwd")
    late = dict(zip(LATE_GATHER, late_bufs))
    half_rows = late["w_ffn_in"].shape[2]
    proj, (late["w_ffn_out"], late["w_out"], late["w_o_attn"]) = _matmul(
        h, wts["w_in"], mode="nn", tm=512, tn=1920, tk=1024, out_dtype=F32, name="mm_proj", b_blocked=True,
        rider=_gather_ici_rider([late["w_ffn_out"], late["w_out"], late["w_o_attn"]], []))
    (qn, kn), (late["w_pw_conv"], late["w_ffn_in"]) = _qk_fwd(
        proj, pos_col, qw2, kw2, consts, "qk_fwd",
        rider=_gather_ici_rider([late["w_pw_conv"], late["w_ffn_in"]], [],
                                row_ranges=[None, (0, half_rows // 2)]))
    (attn, lse), (late["w_ffn_in"],) = _attn_fwd(
        qn, kn, proj, "attn_fwd",
        rider=_gather_ici_rider([late["w_ffn_in"]], [], row_ranges=[(half_rows // 2, half_rows // 2)]))
    (u0, uc), late_bufs = _conv_fwd(proj, wts["conv_w"], wts["conv_b"], conv_col0, "conv_fwd",
                                    rider=_gather_forward_rider([late[n] for n in LATE_GATHER]))
    for n, buf in zip(LATE_GATHER, late_bufs):
        full = buf.reshape(N_CHIPS, -1, buf.shape[3])
        wts[n] = full.reshape(-1, full.shape[2]) if n in ROW_SHARDED else full
    attn_b = attn.astype(BF16)
    y_a = _matmul(attn_b, wts["w_o_attn"], mode="nn", tm=1024, tn=256, tk=512, out_dtype=F32, name="mm_ya",
                  b_blocked=True)
    u3 = _ln_silu_fwd(uc, wts["conv_ln_w"], wts["conv_ln_b"], "ln_fwd")
    y_b = _matmul(u3, wts["w_pw_conv"], mode="nn", tm=1024, tn=256, tk=512, out_dtype=F32, name="mm_yb",
                  b_blocked=True)
    mixed = _gate_fwd(proj, wts["b_gate"], y_a, y_b, gate_col0, "gate_fwd")
    x1 = _matmul(mixed, wts["w_out"], mode="nn", tm=512, tn=1024, tk=1024, out_dtype=F32, name="mm_x1",
                 residual=x)
    h2 = _rmsnorm_fwd(x1, wts["norm2_w"], "rms2_fwd")
    gate, up, act = _ffn_in_swiglu(h2, wts["w_ffn_in"], "mm_gu_swiglu")
    x2 = _matmul(act, wts["w_ffn_out"], mode="nn", tm=512, tn=1024, tk=2816, out_dtype=F32, name="mm_x2",
                 residual=x1)
    dy, dy_b16, loss = _loss_fwd_bwd(x2, target, "loss")

    g = {}
    by_chip = {}

    def pair_add(n, blocks, received):
        return _add_own_half(blocks, received, pos_arr, f"grads_pair_add_{n}")

    d_act = _matmul(dy_b16, wts["w_ffn_out"], mode="nt", tm=512, tn=1408, tk=1024, out_dtype=F32, name="mm_dact")
    g_ffn_out = _blocks_by_half(
        _matmul(act, dy_b16, mode="tn", tm=1408, tn=1024, tk=2048, out_dtype=F32, name="mm_dwffnout"))
    dgu, (received,) = _swiglu_bwd(gate, up, d_act, "swiglu_bwd",
                                   rider=_pair_exchange_rider([g_ffn_out], halved=True))
    to_send, own = pair_add("w_ffn_out", g_ffn_out, received)
    dh2, (by_chip["w_ffn_out"],) = _matmul(
        dgu, wts["w_ffn_in"], mode="nt", tm=1024, tn=1024, tk=1408, out_dtype=F32, name="mm_dh2", b_blocked=True,
        rider=_chip_exchange_rider([to_send], [own]))
    g_ffn_in = _blocks_by_half(_matmul(h2, dgu, mode="tn", tm=512, tn=1408, tk=2048, out_dtype=F32,
                                       name="mm_dwffnin", out_blocked=N_CHIPS))
    dx1, dx1_b16, g["norm2_w"] = _rmsnorm_bwd(dh2, x1, wts["norm2_w"], dy, "rms2_bwd")
    d_mixed = _matmul(dx1_b16, wts["w_out"], mode="nt", tm=512, tn=1024, tk=1024, out_dtype=F32, name="mm_dmixed")
    g["w_out"] = _matmul(mixed, dx1_b16, mode="tn", tm=512, tn=1024, tk=2048, out_dtype=F32, name="mm_dwout")
    (dproj, dy_a, dy_b, db_a, db_b), (received,) = _gate_bwd(
        d_mixed, proj, wts["b_gate"], y_a, y_b, gate_col0, "gate_bwd",
        rider=_pair_exchange_rider([g_ffn_in], halved=True))
    ffn_in_to_send, ffn_in_own = pair_add("w_ffn_in", g_ffn_in, received)
    g["b_gate"] = jnp.concatenate([db_a, db_b], axis=0)
    dattn = _matmul(dy_a, wts["w_o_attn"], mode="nt", tm=1024, tn=512, tk=256, out_dtype=F32, name="mm_dattn",
                    b_blocked=True)
    g["w_o_attn"] = _matmul(attn_b, dy_a, mode="tn", tm=512, tn=256, tk=2048, out_dtype=F32, name="mm_dwo",
                            out_blocked=N_CHIPS)
    du3 = _matmul(dy_b, wts["w_pw_conv"], mode="nt", tm=1024, tn=512, tk=256, out_dtype=F32, name="mm_du3",
                  b_blocked=True)
    g["w_pw_conv"] = _matmul(u3, dy_b, mode="tn", tm=512, tn=256, tk=2048, out_dtype=F32, name="mm_dwpw",
                             out_blocked=N_CHIPS)
    duc, g["conv_ln_w"], g["conv_ln_b"] = _ln_silu_bwd(du3, uc, wts["conv_ln_w"], wts["conv_ln_b"], "ln_bwd")

    small3 = ("w_out", "w_o_attn", "w_pw_conv")
    g_small3 = [_blocks_by_half(g.pop(n)) for n in small3]
    (da, db, g["conv_w"], g["conv_b"]), received = _conv_bwd(
        duc, u0, proj, wts["conv_w"], conv_col0, "conv_bwd", rider=_pair_exchange_rider(g_small3, halved=True))
    sums3 = [pair_add(n, gb, rv) for n, gb, rv in zip(small3, g_small3, received)]
    (dqn, dkn, dv), (by_chip["w_ffn_in"],) = _attn_bwd(
        qn, kn, proj, dattn, attn, lse, bd, "attn_bwd",
        rider=_chip_exchange_rider([ffn_in_to_send], [ffn_in_own]))
    (dproj, dqw, dkw), exchanged3 = _qk_bwd(
        dproj, dqn, dkn, dv, da, db, proj, pos_col, qw2, kw2, consts, "qk_bwd",
        rider=_chip_exchange_rider([s[0] for s in sums3], [s[1] for s in sums3]))
    by_chip.update(zip(small3, exchanged3))
    halves = [_sum_chips(by_chip[n], pos_arr, f"grads_chip_sum_{n}") for n in EARLY_REDUCE]
    g["q_norm_w"] = dqw[:, :HEAD_DIM]
    g["k_norm_w"] = dkw[:, :HEAD_DIM]

    c = pos_arr[0]
    rh = h.shape[1] // 2
    h_sibling = lax.dynamic_slice_in_dim(h, (1 - c) * rh, rh, axis=1)
    h_own = lax.dynamic_slice_in_dim(h, c * rh, rh, axis=1)
    g_sibling, shards = _matmul(h_sibling, dproj, mode="tn", tm=rh, tn=1920, tk=2048, out_dtype=F32,
                                name="mm_dwin_sibling", out_blocked=N_CHIPS, rider=_pair_gather_rider(halves))
    reduced = dict(zip(EARLY_REDUCE, shards))
    g_own, from_sibling = _matmul(h_own, dproj, mode="tn", tm=rh, tn=1920, tk=2048, out_dtype=F32,
                                  name="mm_dwin_own", out_blocked=N_CHIPS,
                                  rider=_pair_exchange_rider([g_sibling], halved=False))
    to_send, own = _add_own_half(g_own, from_sibling[0], pos_arr, "grads_pair_add_w_in")
    dh, by_chip_w_in = _matmul(dproj, wts["w_in"], mode="nt", tm=1024, tn=1024, tk=1920, out_dtype=F32, name="mm_dh",
                               b_blocked=True, rider=_chip_exchange_rider([to_send], [own]))
    grad_x, _, g["norm1_w"] = _rmsnorm_bwd(dh, x, wts["norm1_w"], dx1, "rms1_bwd")
    return loss, grad_x, g, reduced, by_chip_w_in[0]


def _mesh_pos():
    return lax.axis_index("x"), lax.axis_index("y"), lax.axis_index("c")


def _other_chips(x, y):
    return [(1 - x, y), (x, 1 - y), (1 - x, 1 - y)]


def _cast_into_slot(shard, chip_arr, dtype, name):
    r, c = shard.shape
    tr = r // 2 if r % 32 == 0 else r

    def body(chip_ref, s_ref, o_ref):
        del chip_ref
        o_ref[...] = s_ref[...].astype(dtype)

    return pl.pallas_call(
        body, name=name,
        grid_spec=pltpu.PrefetchScalarGridSpec(
            num_scalar_prefetch=1, grid=(r // tr,),
            in_specs=[pl.BlockSpec((tr, c), lambda i, chip_ref: (i, 0))],
            out_specs=pl.BlockSpec((None, tr, c), lambda i, chip_ref: (chip_ref[0], i, 0))),
        out_shape=jax.ShapeDtypeStruct((N_CHIPS, r, c), dtype), compiler_params=_params(),
    )(chip_arr, shard)


def _allgather_inplace(big, small, name):
    nb, ns = len(big), len(small)
    n = nb + ns

    def body(*refs):
        bufs = refs[n:2 * n]
        send_sems, recv_sems, fsend_sems, frecv_sems = refs[2 * n:]
        x, y, c = _mesh_pos()
        me = 2 * x + y
        chips = _other_chips(x, y)

        def part(a, slot, half):
            return bufs[a].at[slot, half] if a < nb else bufs[a].at[slot]

        sends = []
        for a in range(n):
            for k, (px, py) in enumerate(chips):
                cp = pltpu.make_async_remote_copy(
                    src_ref=part(a, me, c), dst_ref=part(a, me, c), send_sem=send_sems.at[a, k],
                    recv_sem=recv_sems.at[a, k], device_id=(px, py, c), device_id_type=MESH)
                cp.start()
                sends.append(cp)
        for a in range(n):
            for k, (px, py) in enumerate(chips):
                slot = 2 * px + py
                pltpu.make_async_remote_copy(
                    src_ref=part(a, slot, c), dst_ref=part(a, slot, c), send_sem=send_sems.at[a, k],
                    recv_sem=recv_sems.at[a, k], device_id=(px, py, c), device_id_type=MESH).wait_recv()
                if a < nb:
                    fwd = pltpu.make_async_remote_copy(
                        src_ref=part(a, slot, c), dst_ref=part(a, slot, c), send_sem=fsend_sems.at[a, k],
                        recv_sem=frecv_sems.at[a, k], device_id=(x, y, 1 - c), device_id_type=MESH)
                    fwd.start()
                    sends.append(fwd)
        for a in range(nb):
            for k, (px, py) in enumerate(chips):
                slot = 2 * px + py
                pltpu.make_async_remote_copy(
                    src_ref=part(a, slot, 1 - c), dst_ref=part(a, slot, 1 - c), send_sem=fsend_sems.at[a, k],
                    recv_sem=frecv_sems.at[a, k], device_id=(x, y, 1 - c), device_id_type=MESH).wait_recv()
        for cp in sends:
            cp.wait_send()

    ops = list(big) + list(small)
    return pl.pallas_call(
        body, name=name, in_specs=[ANY] * n, out_specs=[ANY] * n,
        out_shape=[jax.ShapeDtypeStruct(o.shape, o.dtype) for o in ops],
        input_output_aliases={i: i for i in range(n)},
        scratch_shapes=[pltpu.SemaphoreType.DMA((n, 3)), pltpu.SemaphoreType.DMA((n, 3)),
                        pltpu.SemaphoreType.DMA((nb, 3)), pltpu.SemaphoreType.DMA((nb, 3))],
    )(*ops)


def _comm_call(rider, name):
    def body():
        pass

    return _pallas(body, name=name, grid=(1,), in_specs=[], out_specs=[], out_shape=[], operands=[],
                   rider=rider)[1]


def _gather_ici_rider(big, small, row_ranges=None):
    nb = len(big)
    n = nb + len(small)
    row_ranges = row_ranges or [None] * nb

    def copies(bufs, sems):
        x, y, c = _mesh_pos()
        me = 2 * x + y

        def part(a, slot):
            if a >= nb:
                return bufs[a].at[slot]
            if row_ranges[a] is None:
                return bufs[a].at[slot, c]
            return bufs[a].at[slot, c, pl.ds(*row_ranges[a])]
        out = []
        for a in range(n):
            for k, (px, py) in enumerate(_other_chips(x, y)):
                send = functools.partial(
                    pltpu.make_async_remote_copy,
                    src_ref=part(a, me), dst_ref=part(a, me), send_sem=sems[0].at[a, k],
                    recv_sem=sems[1].at[a, k], device_id=(px, py, c), device_id_type=MESH)
                recv = functools.partial(
                    pltpu.make_async_remote_copy,
                    src_ref=part(a, 2 * px + py), dst_ref=part(a, 2 * px + py), send_sem=sems[0].at[a, k],
                    recv_sem=sems[1].at[a, k], device_id=(px, py, c), device_id_type=MESH)
                out.append((send, recv))
        return out

    def start(r_in, r_out, sems):
        for send, _ in copies(r_out, sems):
            send().start()

    def wait(r_in, r_out, sems):
        cps = copies(r_out, sems)
        for _, recv in cps:
            recv().wait_recv()
        for send, _ in cps:
            send().wait_send()

    ops = list(big) + list(small)
    return _Rider(ops, [jax.ShapeDtypeStruct(o.shape, o.dtype) for o in ops], {i: i for i in range(n)},
                  [pltpu.SemaphoreType.DMA((n, 3)), pltpu.SemaphoreType.DMA((n, 3))], start, wait)


def _gather_forward_rider(big):
    n = len(big)

    def copies(bufs, sems):
        x, y, c = _mesh_pos()
        out = []
        for a in range(n):
            for k, (px, py) in enumerate(_other_chips(x, y)):
                slot = 2 * px + py
                send = functools.partial(
                    pltpu.make_async_remote_copy,
                    src_ref=bufs[a].at[slot, c], dst_ref=bufs[a].at[slot, c], send_sem=sems[0].at[a, k],
                    recv_sem=sems[1].at[a, k], device_id=(x, y, 1 - c), device_id_type=MESH)
                recv = functools.partial(
                    pltpu.make_async_remote_copy,
                    src_ref=bufs[a].at[slot, 1 - c], dst_ref=bufs[a].at[slot, 1 - c], send_sem=sems[0].at[a, k],
                    recv_sem=sems[1].at[a, k], device_id=(x, y, 1 - c), device_id_type=MESH)
                out.append((send, recv))
        return out

    def start(r_in, r_out, sems):
        for send, _ in copies(r_out, sems):
            send().start()

    def wait(r_in, r_out, sems):
        cps = copies(r_out, sems)
        for _, recv in cps:
            recv().wait_recv()
        for send, _ in cps:
            send().wait_send()

    return _Rider(big, [jax.ShapeDtypeStruct(o.shape, o.dtype) for o in big], {i: i for i in range(n)},
                  [pltpu.SemaphoreType.DMA((n, 3)), pltpu.SemaphoreType.DMA((n, 3))], start, wait)


def _pair_exchange_rider(gs, halved):
    n = len(gs)

    def copies(r_in, r_out, sems):
        x, y, c = _mesh_pos()
        return [pltpu.make_async_remote_copy(
            src_ref=r_in[a].at[:, 1 - c] if halved else r_in[a], dst_ref=r_out[a], send_sem=sems[0].at[a],
            recv_sem=sems[1].at[a], device_id=(x, y, 1 - c), device_id_type=MESH) for a in range(n)]

    def start(r_in, r_out, sems):
        for cp in copies(r_in, r_out, sems):
            cp.start()

    def wait(r_in, r_out, sems):
        for cp in copies(r_in, r_out, sems):
            cp.wait()

    return _Rider(gs, [jax.ShapeDtypeStruct((g.shape[0],) + g.shape[-2:], g.dtype) for g in gs], {},
                  [pltpu.SemaphoreType.DMA((n,)), pltpu.SemaphoreType.DMA((n,))], start, wait)


def _chip_exchange_rider(to_send, by_chip):
    n = len(to_send)

    def copies(r_in, r_out, sems):
        x, y, c = _mesh_pos()
        me = 2 * x + y
        out = []
        for a in range(n):
            for k, (px, py) in enumerate(_other_chips(x, y)):
                send = functools.partial(
                    pltpu.make_async_remote_copy,
                    src_ref=r_in[a].at[2 * px + py], dst_ref=r_out[a].at[me], send_sem=sems[0].at[a, k],
                    recv_sem=sems[1].at[a, k], device_id=(px, py, c), device_id_type=MESH)
                recv = functools.partial(
                    pltpu.make_async_remote_copy,
                    src_ref=r_in[a].at[me], dst_ref=r_out[a].at[2 * px + py], send_sem=sems[0].at[a, k],
                    recv_sem=sems[1].at[a, k], device_id=(px, py, c), device_id_type=MESH)
                out.append((send, recv))
        return out

    def start(r_in, r_out, sems):
        for send, _ in copies(r_in, r_out, sems):
            send().start()

    def wait(r_in, r_out, sems):
        cps = copies(r_in, r_out, sems)
        for _, recv in cps:
            recv().wait_recv()
        for send, _ in cps:
            send().wait_send()

    return _Rider(list(to_send) + list(by_chip), [jax.ShapeDtypeStruct(b.shape, b.dtype) for b in by_chip],
                  {n + i: i for i in range(n)},
                  [pltpu.SemaphoreType.DMA((n, 3)), pltpu.SemaphoreType.DMA((n, 3))], start, wait)


def _pair_gather_rider(bufs):
    n = len(bufs)

    def copies(r_out, sems):
        x, y, c = _mesh_pos()
        out = []
        for a in range(n):
            send = functools.partial(
                    pltpu.make_async_remote_copy,
                src_ref=r_out[a].at[c], dst_ref=r_out[a].at[c], send_sem=sems[0].at[a],
                recv_sem=sems[1].at[a], device_id=(x, y, 1 - c), device_id_type=MESH)
            recv = functools.partial(
                    pltpu.make_async_remote_copy,
                src_ref=r_out[a].at[1 - c], dst_ref=r_out[a].at[1 - c], send_sem=sems[0].at[a],
                recv_sem=sems[1].at[a], device_id=(x, y, 1 - c), device_id_type=MESH)
            out.append((send, recv))
        return out

    def start(r_in, r_out, sems):
        for send, _ in copies(r_out, sems):
            send().start()

    def wait(r_in, r_out, sems):
        cps = copies(r_out, sems)
        for _, recv in cps:
            recv().wait_recv()
        for send, _ in cps:
            send().wait_send()

    return _Rider(bufs, [jax.ShapeDtypeStruct(b.shape, b.dtype) for b in bufs], {i: i for i in range(n)},
                  [pltpu.SemaphoreType.DMA((n,)), pltpu.SemaphoreType.DMA((n,))], start, wait)


def _add_own_half(g, recv, pos_arr, name):
    nb, rh, cols = g.shape[0], g.shape[-2], g.shape[-1]

    def body(pos_ref, g_ref, r_ref, send_ref, own_ref):
        s = (g_ref[...] + r_ref[...]).astype(BF16)
        send_ref[...] = s

        @pl.when(pl.program_id(0) == pos_ref[1])
        def _():
            own_ref[...] = s

    blk = pl.BlockSpec((None, rh, cols), lambda j, pos_ref: (j, 0, 0))
    g_spec = blk if g.ndim == 3 else pl.BlockSpec((None, None, rh, cols),
                                                   lambda j, pos_ref: (j, pos_ref[0], 0, 0))
    shape = jax.ShapeDtypeStruct((nb, rh, cols), BF16)
    return pl.pallas_call(
        body, name=name,
        grid_spec=pltpu.PrefetchScalarGridSpec(
            num_scalar_prefetch=1, grid=(nb,), in_specs=[g_spec, blk],
            out_specs=[blk, pl.BlockSpec((None, rh, cols), lambda j, pos_ref: (pos_ref[1], 0, 0))]),
        out_shape=[shape, shape], compiler_params=_params(),
    )(pos_arr, g, recv)


def _sum_chips(gath, pos_arr, name):
    nb, rh, cols = gath.shape

    def body(pos_ref, a_ref, b_ref, c_ref, d_ref, o_ref):
        del pos_ref
        o_ref[...] = ((a_ref[...].astype(F32) + b_ref[...].astype(F32)) + c_ref[...].astype(F32)) \
            + d_ref[...].astype(F32)

    tr = rh // 2 if (rh // 2) % 16 == 0 else rh
    specs = [pl.BlockSpec((None, tr, cols), functools.partial(lambda i, pos_ref, j: (j, i, 0), j=j))
             for j in range(nb)]
    return pl.pallas_call(
        body, name=name,
        grid_spec=pltpu.PrefetchScalarGridSpec(
            num_scalar_prefetch=1, grid=(rh // tr,), in_specs=specs,
            out_specs=pl.BlockSpec((None, tr, cols), lambda i, pos_ref: (pos_ref[0], i, 0))),
        out_shape=jax.ShapeDtypeStruct((2, rh, cols), F32), compiler_params=_params(),
    )(pos_arr, gath, gath, gath, gath)


def _small_allreduce(v, name, rider=None):
    n = v.shape[0]
    n_dev = 8

    def body(v_ref, o_ref, buf, send_sems, recv_sems):
        x, y, c = _mesh_pos()
        me = 4 * x + 2 * y + c
        buf[me] = v_ref[...]
        sends = []
        peers = []
        for r in range(1, n_dev):
            px = 1 - x if r & 4 else x
            py = 1 - y if r & 2 else y
            pc = 1 - c if r & 1 else c
            peers.append((px, py, pc))
            cp = pltpu.make_async_remote_copy(
                src_ref=v_ref, dst_ref=buf.at[me], send_sem=send_sems.at[r - 1],
                recv_sem=recv_sems.at[r - 1], device_id=(px, py, pc), device_id_type=MESH)
            cp.start()
            sends.append(cp)
        for r, (px, py, pc) in enumerate(peers):
            pltpu.make_async_remote_copy(
                src_ref=v_ref, dst_ref=buf.at[4 * px + 2 * py + pc], send_sem=send_sems.at[r],
                recv_sem=recv_sems.at[r], device_id=(px, py, pc), device_id_type=MESH).wait_recv()
        for cp in sends:
            cp.wait_send()
        acc = buf[0]
        for i in range(1, n_dev):
            acc = acc + buf[i]
        o_ref[...] = acc

    whole = pl.BlockSpec(v.shape, lambda i: (0, 0))
    return _pallas(
        body, name=name, grid=(1,), in_specs=[whole], out_specs=whole,
        out_shape=jax.ShapeDtypeStruct(v.shape, v.dtype), operands=[v],
        scratch_shapes=[pltpu.VMEM((n_dev, n, LANES), F32), pltpu.SemaphoreType.DMA((n_dev - 1,)),
                        pltpu.SemaphoreType.DMA((n_dev - 1,))],
        rider=rider)


def _adamw_math(w, g, m, v):
    m = ADAM_B1 * m + (1.0 - ADAM_B1) * g
    v = ADAM_B2 * v + (1.0 - ADAM_B2) * (g * g)
    m_hat = m / (1.0 - ADAM_B1 ** ADAM_STEP)
    v_hat = v / (1.0 - ADAM_B2 ** ADAM_STEP)
    delta = -ADAM_LR * (m_hat / (jnp.sqrt(v_hat) + ADAM_EPS) + ADAM_WD * w)
    return delta, m, v


def _adamw(w, g, m, v, name):
    r, c = w.shape
    tr = 128 if r % 128 == 0 else 64
    assert r % tr == 0

    def body(w_ref, g_ref, m_ref, v_ref, go_ref, d_ref, mo_ref, vo_ref):
        gv = g_ref[...]
        d, mn, vn = _adamw_math(w_ref[...], gv, m_ref[...], v_ref[...])
        go_ref[...] = gv
        d_ref[...] = d
        mo_ref[...] = mn
        vo_ref[...] = vn

    blk = pl.BlockSpec((tr, c), lambda i: (i, 0))
    return _pallas(body, name=name, grid=(r // tr,), in_specs=[blk] * 4, out_specs=[blk] * 4,
                   out_shape=[jax.ShapeDtypeStruct((r, c), F32)] * 4, operands=[w, g, m, v])


def _adamw_small(ws, gs, ms, vs, name):
    n = len(ws)

    def body(*refs):
        w_r, g_r, m_r, v_r = refs[:n], refs[n:2 * n], refs[2 * n:3 * n], refs[3 * n:4 * n]
        d_o, m_o, v_o = refs[4 * n:5 * n], refs[5 * n:6 * n], refs[6 * n:7 * n]
        for i in range(n):
            d, mn, vn = _adamw_math(w_r[i][...], g_r[i][...], m_r[i][...], v_r[i][...])
            d_o[i][...] = d
            m_o[i][...] = mn
            v_o[i][...] = vn

    vm = pl.BlockSpec(memory_space=pltpu.VMEM)
    shapes = [jax.ShapeDtypeStruct(w.shape, F32) for w in ws]
    outs = pl.pallas_call(
        body, name=name, in_specs=[vm] * (4 * n), out_specs=[vm] * (3 * n), out_shape=shapes * 3,
        compiler_params=_params(),
    )(*ws, *gs, *ms, *vs)
    return outs[:n], outs[n:2 * n], outs[2 * n:]


BIG = ("w_in", "w_o_attn", "w_pw_conv", "w_out", "w_ffn_in", "w_ffn_out")
ROW_SHARDED = ("w_out", "w_ffn_out")
SMALL = ("norm1_w", "b_gate", "q_norm_w", "k_norm_w", "conv_w", "conv_b", "conv_ln_w", "conv_ln_b", "norm2_w")
ORDER = ("norm1_w", "w_in", "b_gate", "q_norm_w", "k_norm_w", "w_o_attn", "conv_w", "conv_b", "conv_ln_w",
         "conv_ln_b", "w_pw_conv", "w_out", "norm2_w", "w_ffn_in", "w_ffn_out")
PACK_TILE = 8 * LANES


def _pack_small(parts):
    rows = []
    for p in parts:
        flat = p.reshape(-1)
        pad = (-flat.shape[0]) % PACK_TILE
        rows.append(jnp.pad(flat, (0, pad)).reshape(-1, LANES))
    return jnp.concatenate(rows, axis=0)


def _unpack_small(packed, shapes):
    out, row = [], 0
    for shp in shapes:
        size = int(np.prod(shp))
        nrow = -(-size // PACK_TILE) * (PACK_TILE // LANES)
        out.append(packed[row:row + nrow].reshape(-1)[:size].reshape(shp))
        row += nrow
    return out


def kernel(x, positions, norm1_w, w_in, b_gate, q_norm_w, k_norm_w, w_o_attn, conv_w, conv_b, conv_ln_w, conv_ln_b, w_pw_conv, w_out, norm2_w, w_ffn_in, w_ffn_out, loss_target, m_norm1_w, m_w_in, m_b_gate, m_q_norm_w, m_k_norm_w, m_w_o_attn, m_conv_w, m_conv_b, m_conv_ln_w, m_conv_ln_b, m_w_pw_conv, m_w_out, m_norm2_w, m_w_ffn_in, m_w_ffn_out, v_norm1_w, v_w_in, v_b_gate, v_q_norm_w, v_k_norm_w, v_w_o_attn, v_conv_w, v_conv_b, v_conv_ln_w, v_conv_ln_b, v_w_pw_conv, v_w_out, v_norm2_w, v_w_ffn_in, v_w_ffn_out):
    w = dict(norm1_w=norm1_w, w_in=w_in, b_gate=b_gate, q_norm_w=q_norm_w, k_norm_w=k_norm_w, w_o_attn=w_o_attn,
             conv_w=conv_w, conv_b=conv_b, conv_ln_w=conv_ln_w, conv_ln_b=conv_ln_b, w_pw_conv=w_pw_conv,
             w_out=w_out, norm2_w=norm2_w, w_ffn_in=w_ffn_in, w_ffn_out=w_ffn_out)
    m = dict(norm1_w=m_norm1_w, w_in=m_w_in, b_gate=m_b_gate, q_norm_w=m_q_norm_w, k_norm_w=m_k_norm_w,
             w_o_attn=m_w_o_attn, conv_w=m_conv_w, conv_b=m_conv_b, conv_ln_w=m_conv_ln_w,
             conv_ln_b=m_conv_ln_b, w_pw_conv=m_w_pw_conv, w_out=m_w_out, norm2_w=m_norm2_w,
             w_ffn_in=m_w_ffn_in, w_ffn_out=m_w_ffn_out)
    v = dict(norm1_w=v_norm1_w, w_in=v_w_in, b_gate=v_b_gate, q_norm_w=v_q_norm_w, k_norm_w=v_k_norm_w,
             w_o_attn=v_w_o_attn, conv_w=v_conv_w, conv_b=v_conv_b, conv_ln_w=v_conv_ln_w,
             conv_ln_b=v_conv_ln_b, w_pw_conv=v_w_pw_conv, w_out=v_w_out, norm2_w=v_norm2_w,
             w_ffn_in=v_w_ffn_in, w_ffn_out=v_w_ffn_out)
    cx, cy, cc = _mesh_pos()
    chip = 2 * cx + cy

    chip_arr = chip.reshape(1).astype(jnp.int32)
    pos_arr = jnp.stack([cc, chip]).astype(jnp.int32)
    bufs = {}
    for n in BIG:
        buf = _cast_into_slot(w[n][0], chip_arr, BF16, f"cast_{n}")
        bufs[n] = buf.reshape(N_CHIPS, 2, buf.shape[1] // 2, buf.shape[2])
    small_bufs = [_cast_into_slot(w[n][0], chip_arr, F32, f"slot_{n}") for n in ("conv_w", "b_gate")]
    w_in_buf, conv_w_buf, b_gate_buf = _allgather_inplace([bufs["w_in"]], small_bufs, "allgather_w_in")
    wts = dict(w_in=w_in_buf.reshape(N_CHIPS, -1, w_in_buf.shape[3]),
               conv_w=conv_w_buf.transpose(1, 0, 2).reshape(CONV_WIDTH, -1),
               b_gate=b_gate_buf.transpose(1, 0, 2).reshape(2, 1, -1),
               norm1_w=norm1_w, q_norm_w=q_norm_w, k_norm_w=k_norm_w, conv_b=conv_b, conv_ln_w=conv_ln_w,
               conv_ln_b=conv_ln_b, norm2_w=norm2_w)

    loss, grad_x, g, reduced, by_chip_w_in = _forward_backward(
        x[0], positions.reshape(-1, 1), loss_target[0], wts, [bufs[n] for n in LATE_GATHER], pos_arr)
    grads = {n: b.reshape(-1, b.shape[2]) for n, b in reduced.items()}
    half_w_in = _sum_chips(by_chip_w_in, pos_arr, "grads_chip_sum_w_in")

    small_parts = [loss] + [g[n] for n in SMALL]
    small_shapes = [p.shape for p in small_parts]
    summed, (shard_w_in,) = _small_allreduce(_pack_small(small_parts), "small_allreduce",
                                             rider=_pair_gather_rider([half_w_in]))
    grads["w_in"] = shard_w_in.reshape(-1, shard_w_in.shape[2])
    reduced = _unpack_small(summed, small_shapes)
    loss_total = reduced[0].reshape(())
    for n, r in zip(SMALL, reduced[1:]):
        grads[n] = r
    ch_shard = conv_w.shape[2]
    grads["conv_w"] = lax.dynamic_slice_in_dim(grads["conv_w"], chip * ch_shard, ch_shard, axis=1)
    d_shard = b_gate.shape[2]
    grads["b_gate"] = lax.dynamic_slice_in_dim(grads["b_gate"], chip * d_shard, d_shard, axis=1)

    delta, new_m, new_v = {}, {}, {}
    for n in BIG:
        grads[n], delta[n], new_m[n], new_v[n] = _adamw(w[n][0], grads[n], m[n][0], v[n][0], f"adamw_{n}")
    flat2 = lambda a: a.reshape(-1, a.shape[-1])
    d_s, m_s, v_s = _adamw_small([flat2(w[n]) for n in SMALL], [flat2(grads[n]) for n in SMALL],
                                 [flat2(m[n]) for n in SMALL], [flat2(v[n]) for n in SMALL], "adamw_small")
    for i, n in enumerate(SMALL):
        delta[n], new_m[n], new_v[n] = d_s[i], m_s[i], v_s[i]

    shaped = lambda d, n: d[n].reshape(w[n].shape)
    return (loss_total, grad_x[None], *[shaped(grads, n) for n in ORDER], *[shaped(delta, n) for n in ORDER],
            *[shaped(new_m, n) for n in ORDER], *[shaped(new_v, n) for n in ORDER])
```

```python
import functools

import numpy as np
import jax
import jax.numpy as jnp
from jax import lax
from jax.experimental import pallas as pl
from jax.experimental.pallas import tpu as pltpu

F32 = jnp.float32
BF16 = jnp.bfloat16
MESH = pl.DeviceIdType.MESH
ANY = pl.BlockSpec(memory_space=pl.ANY)

HEAD_DIM = 64
N_SLOT_HEADS = 8
DILATIONS = (1, 4, 16)
HALF_SPAN = 64
ROPE_THETA = 500000.0
ROT_DIM = 16
CONV_WIDTH = 31
EPS = 1e-6
NEG_INF = -1e30
ADAM_LR, ADAM_B1, ADAM_B2, ADAM_EPS, ADAM_WD, ADAM_STEP = 0.001, 0.9, 0.999, 1e-08, 0.01, 10

LANES = 128
QBLK = 128
KWIN = QBLK + 2 * HALF_SPAN
VMEM_LIMIT = 48 * 1024 * 1024
N_CHIPS = 4
HIGHEST = lax.Precision.HIGHEST


def _params(**kw):
    return pltpu.CompilerParams(vmem_limit_bytes=VMEM_LIMIT, **kw)


class _Rider:
    def __init__(self, operands, out_shapes, aliases, scratch, start, wait):
        self.operands, self.out_shapes, self.aliases = list(operands), list(out_shapes), dict(aliases)
        self.scratch, self.start, self.wait = list(scratch), start, wait


def _pallas(body, *, name, grid, in_specs, out_specs, out_shape, operands, scratch_shapes=(), aliases=None,
            rider=None):
    single = not isinstance(out_specs, (list, tuple))
    out_specs_l = [out_specs] if single else list(out_specs)
    out_shape_l = [out_shape] if single else list(out_shape)
    aliases = dict(aliases or {})
    if rider is None:
        res = pl.pallas_call(
            body, name=name, grid=grid, in_specs=list(in_specs), out_specs=out_specs_l, out_shape=out_shape_l,
            scratch_shapes=list(scratch_shapes), input_output_aliases=aliases, compiler_params=_params(),
        )(*operands)
        return res[0] if single else res
    n_in, n_rin = len(in_specs), len(rider.operands)
    n_out, n_rout = len(out_specs_l), len(rider.out_shapes)
    n_sc = len(scratch_shapes)

    def wrapped(*refs):
        main_in, r_in = refs[:n_in], refs[n_in:n_in + n_rin]
        o0 = n_in + n_rin
        main_out, r_out = refs[o0:o0 + n_out], refs[o0 + n_out:o0 + n_out + n_rout]
        s0 = o0 + n_out + n_rout
        main_sc, r_sc = refs[s0:s0 + n_sc], refs[s0 + n_sc:]
        ids = [pl.program_id(d) for d in range(len(grid))]
        first = functools.reduce(jnp.logical_and, [i == 0 for i in ids])
        last = functools.reduce(jnp.logical_and, [i == n - 1 for i, n in zip(ids, grid)])

        @pl.when(first)
        def _():
            rider.start(r_in, r_out, r_sc)

        body(*main_in, *main_out, *main_sc)

        @pl.when(last)
        def _():
            rider.wait(r_in, r_out, r_sc)

    for src, dst in rider.aliases.items():
        aliases[n_in + src] = n_out + dst
    res = pl.pallas_call(
        wrapped, name=name, grid=grid, in_specs=list(in_specs) + [ANY] * n_rin,
        out_specs=out_specs_l + [ANY] * n_rout, out_shape=out_shape_l + rider.out_shapes,
        scratch_shapes=list(scratch_shapes) + rider.scratch, input_output_aliases=aliases,
        compiler_params=_params(),
    )(*operands, *rider.operands)
    main = res[:n_out]
    return (main[0] if single else main), res[n_out:]


def _matmul(a, b, *, mode, tm, tn, tk, out_dtype, name, b_blocked=False,
            out_blocked=None, residual=None, rider=None):
    a_shape = a.shape
    if mode == "nn":
        m_dim, k_dim = a_shape
        n_dim = b.shape[0] * b.shape[2] if b_blocked else b.shape[1]
        rows, cols, red = m_dim, n_dim, k_dim
    elif mode == "nt":
        m_dim, n_dim = a_shape
        k_dim = b.shape[1] if b_blocked else b.shape[0]
        rows, cols, red = m_dim, k_dim, n_dim
    else:
        m_dim, k_dim = a_shape
        n_dim = b.shape[1]
        rows, cols, red = k_dim, n_dim, m_dim
    assert rows % tm == 0 and cols % tn == 0 and red % tk == 0, (name, rows, cols, red)
    ni, nj, nk = rows // tm, cols // tn, red // tk

    if mode == "nn":
        a_spec = pl.BlockSpec((tm, tk), lambda i, j, k: (i, k))
        if b_blocked:
            per = b.shape[2] // tn
            b_spec = pl.BlockSpec((None, tk, tn), lambda i, j, k: (j // per, k, j % per))
        else:
            b_spec = pl.BlockSpec((tk, tn), lambda i, j, k: (k, j))
        dims = (((1,), (0,)), ((), ()))
    elif mode == "nt":
        a_spec = pl.BlockSpec((tm, tk), lambda i, j, k: (i, k))
        if b_blocked:
            per = b.shape[2] // tk
            b_spec = pl.BlockSpec((None, tn, tk), lambda i, j, k: (k // per, j, k % per))
        else:
            b_spec = pl.BlockSpec((tn, tk), lambda i, j, k: (j, k))
        dims = (((1,), (1,)), ((), ()))
    else:
        a_spec = pl.BlockSpec((tk, tm), lambda i, j, k: (k, i))
        b_spec = pl.BlockSpec((tk, tn), lambda i, j, k: (k, j))
        dims = (((0,), (0,)), ((), ()))

    if out_blocked:
        per_o = (cols // out_blocked) // tn
        out_spec = pl.BlockSpec((None, tm, tn), lambda i, j, k: (j // per_o, i, j % per_o))
        out_shape = jax.ShapeDtypeStruct((out_blocked, rows, cols // out_blocked), out_dtype)
    else:
        out_spec = pl.BlockSpec((tm, tn), lambda i, j, k: (i, j))
        out_shape = jax.ShapeDtypeStruct((rows, cols), out_dtype)

    in_specs = [a_spec, b_spec]
    operands = [a, b]
    if residual is not None:
        in_specs.append(pl.BlockSpec((tm, tn), lambda i, j, k: (i, j)))
        operands.append(residual)
    has_res = residual is not None

    def body(*refs):
        a_ref, b_ref = refs[0], refs[1]
        res_ref = refs[2] if has_res else None
        o_ref = refs[3] if has_res else refs[2]
        prod = lax.dot_general(a_ref[...], b_ref[...], dims, preferred_element_type=F32)

        def finish(val):
            if has_res:
                val = val + res_ref[...]
            o_ref[...] = val.astype(out_dtype)

        if nk == 1:
            finish(prod)
        else:
            acc_ref = refs[-1]
            k = pl.program_id(2)

            @pl.when(k == 0)
            def _():
                acc_ref[...] = prod

            @pl.when(k > 0)
            def _():
                acc_ref[...] += prod

            @pl.when(k == nk - 1)
            def _():
                finish(acc_ref[...])

    scratch = [pltpu.VMEM((tm, tn), F32)] if nk > 1 else []
    return _pallas(body, name=name, grid=(ni, nj, nk), in_specs=in_specs, out_specs=out_spec,
                   out_shape=out_shape, operands=operands, scratch_shapes=scratch, rider=rider)


def _rmsnorm_fwd(x, w, name):
    s, d = x.shape
    tm = 256

    def body(x_ref, w_ref, o_ref):
        xv = x_ref[...]
        rstd = lax.rsqrt(jnp.mean(xv * xv, axis=-1, keepdims=True) + EPS)
        o_ref[...] = (xv * rstd * w_ref[...]).astype(BF16)

    return pl.pallas_call(
        body, name=name, grid=(s // tm,),
        in_specs=[pl.BlockSpec((tm, d), lambda i: (i, 0)), pl.BlockSpec((1, d), lambda i: (0, 0))],
        out_specs=pl.BlockSpec((tm, d), lambda i: (i, 0)),
        out_shape=jax.ShapeDtypeStruct((s, d), BF16), compiler_params=_params(),
    )(x, w)


def _rmsnorm_bwd(dh, x, w, dres, name, rider=None):
    s, d = x.shape
    tm = 256

    def body(dh_ref, x_ref, w_ref, dres_ref, dx_ref, dxb_ref, dw_ref):
        xv = x_ref[...]
        rstd = lax.rsqrt(jnp.mean(xv * xv, axis=-1, keepdims=True) + EPS)
        xhat = xv * rstd
        dhv = dh_ref[...]
        g = dhv * w_ref[...]
        dx = rstd * (g - xhat * jnp.mean(g * xhat, axis=-1, keepdims=True)) + dres_ref[...]
        dx_ref[...] = dx
        dxb_ref[...] = dx.astype(BF16)
        part = jnp.sum(dhv * xhat, axis=0, keepdims=True)

        @pl.when(pl.program_id(0) == 0)
        def _():
            dw_ref[...] = part

        @pl.when(pl.program_id(0) > 0)
        def _():
            dw_ref[...] += part

    row = pl.BlockSpec((tm, d), lambda i: (i, 0))
    vec = pl.BlockSpec((1, d), lambda i: (0, 0))
    return _pallas(
        body, name=name, grid=(s // tm,), in_specs=[row, row, vec, row], out_specs=[row, row, vec],
        out_shape=[jax.ShapeDtypeStruct((s, d), F32), jax.ShapeDtypeStruct((s, d), BF16),
                   jax.ShapeDtypeStruct((1, d), F32)],
        operands=[dh, x, w, dres], rider=rider)


def _rope_consts():
    lane = np.arange(LANES)
    in_head = lane % HEAD_DIM
    inv_freq = ROPE_THETA ** (-jnp.arange(0, ROT_DIM, 2, dtype=F32) / ROT_DIM)
    invf = jnp.where(jnp.asarray(in_head < ROT_DIM), jnp.tile(inv_freq, LANES // (ROT_DIM // 2)), 0.0)
    m_a = np.where(in_head < ROT_DIM // 2, -1.0, 0.0).astype(np.float32)
    m_b = np.where((in_head >= ROT_DIM // 2) & (in_head < ROT_DIM), 1.0, 0.0).astype(np.float32)
    block_diag = (lane[:, None] // HEAD_DIM == lane[None, :] // HEAD_DIM).astype(np.float32)
    return (invf.reshape(1, LANES).astype(F32), jnp.asarray(m_a).reshape(1, LANES),
            jnp.asarray(m_b).reshape(1, LANES), jnp.asarray(block_diag))


def _head_sums(v, bd):
    return jnp.dot(v, bd, precision=HIGHEST, preferred_element_type=F32)


def _qk_fwd(proj, pos_col, qw2, kw2, consts, name, rider=None):
    s = proj.shape[0]
    width = 3 * N_SLOT_HEADS * HEAD_DIM
    tm = 256
    invf, m_a, m_b, bd = consts
    scale = HEAD_DIM ** -0.5

    def body(q_ref, k_ref, pos_ref, qw_ref, kw_ref, invf_ref, ma_ref, mb_ref, bd_ref, qo_ref, ko_ref):
        ang = pos_ref[...].astype(F32) * invf_ref[...]
        cos = jnp.cos(ang)
        sin = jnp.sin(ang)
        s_a = sin * ma_ref[...]
        s_b = sin * mb_ref[...]
        bdv = bd_ref[...]
        for src, w_ref, dst, sc in ((q_ref, qw_ref, qo_ref, scale), (k_ref, kw_ref, ko_ref, 1.0)):
            for cb in range(width // LANES):
                cols = slice(cb * LANES, (cb + 1) * LANES)
                t = src[:, cols]
                rstd = lax.rsqrt(_head_sums(t * t, bdv) * (1.0 / HEAD_DIM) + EPS)
                y = t * rstd * w_ref[...]
                r = y * cos + pltpu.roll(y, LANES - 8, axis=1) * s_a + pltpu.roll(y, 8, axis=1) * s_b
                dst[:, cols] = r * sc if sc != 1.0 else r

    vec = pl.BlockSpec((1, LANES), lambda i: (0, 0))
    return _pallas(
        body, name=name, grid=(s // tm,),
        in_specs=[pl.BlockSpec((tm, width), lambda i: (i, 0)), pl.BlockSpec((tm, width), lambda i: (i, 1)),
                  pl.BlockSpec((tm, 1), lambda i: (i, 0)), vec, vec, vec, vec, vec,
                  pl.BlockSpec((LANES, LANES), lambda i: (0, 0))],
        out_specs=[pl.BlockSpec((tm, width), lambda i: (i, 0))] * 2,
        out_shape=[jax.ShapeDtypeStruct((s, width), F32)] * 2,
        operands=[proj, proj, pos_col, qw2, kw2, invf, m_a, m_b, bd], rider=rider)


def _qk_bwd(dproj, dqn, dkn, dv, da, db, proj, pos_col, qw2, kw2, consts, name, rider=None):
    s = proj.shape[0]
    width = 3 * N_SLOT_HEADS * HEAD_DIM
    ch = da.shape[1]
    out_w = 3 * width + 2 * ch
    tm = 256
    invf, m_a, m_b, bd = consts
    scale = HEAD_DIM ** -0.5

    def body(dproj_in, dq_ref, dk_ref, dv_ref, da_ref, db_ref, q_ref, k_ref, pos_ref, qw_ref, kw_ref,
             invf_ref, ma_ref, mb_ref, bd_ref, out_ref, dqw_ref, dkw_ref):
        del dproj_in
        ang = pos_ref[...].astype(F32) * invf_ref[...]
        cos = jnp.cos(ang)
        sin = jnp.sin(ang)
        s_a = sin * ma_ref[...]
        s_b = sin * mb_ref[...]
        bdv = bd_ref[...]
        first = pl.program_id(0) == 0
        for src, dsrc, w_ref, col0, dw_ref, sc in ((q_ref, dq_ref, qw_ref, 0, dqw_ref, scale),
                                                   (k_ref, dk_ref, kw_ref, width, dkw_ref, 1.0)):
            dw_acc = jnp.zeros((1, LANES), F32)
            for cb in range(width // LANES):
                cols = slice(cb * LANES, (cb + 1) * LANES)
                t = src[:, cols]
                dr = dsrc[:, cols]
                if sc != 1.0:
                    dr = dr * sc
                dy = dr * cos + pltpu.roll(dr * s_a, 8, axis=1) + pltpu.roll(dr * s_b, LANES - 8, axis=1)
                rstd = lax.rsqrt(_head_sums(t * t, bdv) * (1.0 / HEAD_DIM) + EPS)
                xhat = t * rstd
                g = dy * w_ref[...]
                dt = rstd * (g - xhat * (_head_sums(g * xhat, bdv) * (1.0 / HEAD_DIM)))
                out_ref[:, col0 + cb * LANES: col0 + (cb + 1) * LANES] = dt.astype(BF16)
                dw_acc = dw_acc + jnp.sum(dy * xhat, axis=0, keepdims=True)
            dw_acc = dw_acc + pltpu.roll(dw_acc, HEAD_DIM, axis=1)

            @pl.when(first)
            def _(dw_ref=dw_ref, dw_acc=dw_acc):
                dw_ref[...] = dw_acc

            @pl.when(jnp.logical_not(first))
            def _(dw_ref=dw_ref, dw_acc=dw_acc):
                dw_ref[...] += dw_acc
        out_ref[:, 2 * width: 3 * width] = dv_ref[...].astype(BF16)
        out_ref[:, 3 * width: 3 * width + ch] = da_ref[...]
        out_ref[:, 3 * width + ch: out_w] = db_ref[...]

    vec = pl.BlockSpec((1, LANES), lambda i: (0, 0))
    blk = lambda c: pl.BlockSpec((tm, width), lambda i: (i, c))
    cblk = pl.BlockSpec((tm, ch), lambda i: (i, 0))
    return _pallas(
        body, name=name, grid=(s // tm,),
        in_specs=[ANY, blk(0), blk(0), blk(0), cblk, cblk, blk(0), blk(1),
                  pl.BlockSpec((tm, 1), lambda i: (i, 0)), vec, vec, vec, vec, vec,
                  pl.BlockSpec((LANES, LANES), lambda i: (0, 0))],
        out_specs=[pl.BlockSpec((tm, out_w), lambda i: (i, 0)), vec, vec],
        out_shape=[jax.ShapeDtypeStruct(dproj.shape, BF16)] + [jax.ShapeDtypeStruct((1, LANES), F32)] * 2,
        operands=[dproj, dqn, dkn, dv, da, db, proj, proj, pos_col, qw2, kw2, invf, m_a, m_b, bd],
        aliases={0: 0}, rider=rider)


def _row_chunks(n_rows, fn, chunk=256):
    def step(i, c):
        fn(pl.ds(pl.multiple_of(i * chunk, chunk), chunk))
        return c
    lax.fori_loop(0, n_rows // chunk, step, 0)


def _to_residue_major(dst, src, s, d, dst_off=0, cast=None):
    seq = s // d
    for r in range(d):
        v = src[...] if d == 1 else src[pl.ds(r, seq, stride=d), :]
        dst[dst_off + r * seq: dst_off + (r + 1) * seq, :] = v if cast is None else v.astype(cast)


def _from_residue_major(dst, src, s, d, src_off=0):
    seq = s // d
    for r in range(d):
        v = src[src_off + r * seq: src_off + (r + 1) * seq, :]
        if d == 1:
            dst[...] = v
        else:
            dst[pl.ds(r, seq, stride=d), :] = v


def _band_bias():
    qi = lax.broadcasted_iota(jnp.int32, (QBLK, KWIN), 0)
    kj = lax.broadcasted_iota(jnp.int32, (QBLK, KWIN), 1)
    return jnp.where(jnp.abs(kj - HALF_SPAN - qi) <= HALF_SPAN, 0.0, NEG_INF).astype(F32)


def _range_bias(base, seq):
    kj = lax.broadcasted_iota(jnp.int32, (1, KWIN), 1)
    lo = (base & -seq) - base + HALF_SPAN
    return jnp.where((kj >= lo) & (kj < lo + seq), 0.0, NEG_INF).astype(F32)


def _block_base(b):
    return b * QBLK if isinstance(b, int) else pl.multiple_of(b * QBLK, QBLK)


def _attn_fwd(qn, kn, proj, name, rider=None):
    s = qn.shape[0]
    n_pairs = N_SLOT_HEADS * HEAD_DIM // LANES
    v_col0 = 2 * qn.shape[1] // LANES
    nt_dims = (((1,), (1,)), ((), ()))

    def body(q_ref, k_ref, v_ref, attn_ref, lse_ref, q_rm, k_rm, v_rm, acc_rm, m_rm, l_rm,
             acc_p, m_p, l_p, m_run, l_run, acc_run, band, s_buf, m_buf):
        g = pl.program_id(1)
        zpad = jnp.zeros((HALF_SPAN, LANES), BF16)
        k_rm[0:HALF_SPAN, :] = zpad
        k_rm[s + HALF_SPAN: s + 2 * HALF_SPAN, :] = zpad
        v_rm[0:HALF_SPAN, 0:LANES] = zpad
        v_rm[s + HALF_SPAN: s + 2 * HALF_SPAN, 0:LANES] = zpad

        def ones_rows(rows):
            v_rm[pl.ds(rows.start, rows.size), LANES:2 * LANES] = jnp.ones((rows.size, LANES), BF16)

        _row_chunks(s + 2 * HALF_SPAN, ones_rows, chunk=2 * HALF_SPAN)
        band[...] = _band_bias()
        lane = lax.broadcasted_iota(jnp.int32, (QBLK, LANES), 1)
        low = lane < HEAD_DIM
        n_blk = s // QBLK

        for gi, d in enumerate(DILATIONS):
            @pl.when(g == gi)
            def _(gi=gi, d=d):
                seq = s // d
                _to_residue_major(q_rm, q_ref, s, d, cast=BF16)
                _to_residue_major(k_rm, k_ref, s, d, dst_off=HALF_SPAN, cast=BF16)
                _to_residue_major(v_rm.at[:, 0:LANES], v_ref, s, d, dst_off=HALF_SPAN, cast=BF16)

                def scores(b, slot):
                    base = _block_base(b)
                    q = q_rm[pl.ds(base, QBLK), :]
                    zero = jnp.zeros_like(q)
                    q2 = jnp.concatenate([jnp.where(low, q, zero), jnp.where(low, zero, q)], axis=0)
                    sc = lax.dot_general(q2, k_rm[pl.ds(base, KWIN), :], nt_dims, preferred_element_type=F32)
                    bias = band[...] + _range_bias(base, seq)
                    for hh in range(2):
                        rows = slice(hh * QBLK, (hh + 1) * QBLK)
                        sh = sc[rows, :] + bias
                        s_buf[slot, rows, :] = sh
                        m_buf[slot, rows, :] = jnp.broadcast_to(jnp.max(sh, axis=-1, keepdims=True), (QBLK, LANES))

                def outputs(b, slot):
                    base = _block_base(b)
                    sv = s_buf[slot]
                    mb = m_buf[slot]
                    p = jnp.exp(jnp.concatenate([sv[:, 0:LANES] - mb, sv[:, LANES:2 * LANES] - mb], axis=1))
                    pv = jnp.dot(p.astype(BF16), v_rm[pl.ds(base, KWIN), :], preferred_element_type=F32)
                    rows = pl.ds(base, QBLK)
                    acc_rm[rows, :] = jnp.where(low, pv[0:QBLK, 0:LANES], pv[QBLK:2 * QBLK, 0:LANES])
                    l_rm[rows, :] = jnp.where(low, pv[0:QBLK, LANES:2 * LANES], pv[QBLK:2 * QBLK, LANES:2 * LANES])
                    m_rm[rows, :] = jnp.where(low, mb[0:QBLK, :], mb[QBLK:2 * QBLK, :])

                scores(0, 0)

                def pair(i, carry):
                    b = 2 * i
                    outputs(b, 0)
                    scores(b + 1, 1)
                    outputs(b + 1, 1)
                    scores(b + 2, 0)
                    return carry

                lax.fori_loop(0, n_blk // 2 - 1, pair, 0)
                outputs(n_blk - 2, 0)
                scores(n_blk - 1, 1)
                outputs(n_blk - 1, 1)
                if d == 1:
                    src = (acc_rm, m_rm, l_rm)
                else:
                    for dst_, src_ in ((acc_p, acc_rm), (m_p, m_rm), (l_p, l_rm)):
                        _from_residue_major(dst_, src_, s, d)
                    src = (acc_p, m_p, l_p)

                def combine(rows):
                    a_g, m_g, l_g = src[0][rows, :], src[1][rows, :], src[2][rows, :]
                    if gi == 0:
                        m_new, l_new, a_new = m_g, l_g, a_g
                    else:
                        m_old = m_run[rows, :]
                        m_new = jnp.maximum(m_old, m_g)
                        w_old = jnp.exp(m_old - m_new)
                        w_g = jnp.exp(m_g - m_new)
                        l_new = l_run[rows, :] * w_old + l_g * w_g
                        a_new = acc_run[rows, :] * w_old + a_g * w_g
                    if gi == len(DILATIONS) - 1:
                        attn_ref[rows, :] = a_new / l_new
                        lse_ref[rows, :] = m_new + jnp.log(l_new)
                    else:
                        m_run[rows, :] = m_new
                        l_run[rows, :] = l_new
                        acc_run[rows, :] = a_new

                _row_chunks(s, combine)

    qk_spec = pl.BlockSpec((s, LANES), lambda hp, g: (0, g * n_pairs + hp))
    v_spec = pl.BlockSpec((s, LANES), lambda hp, g: (0, v_col0 + g * n_pairs + hp))
    o_spec = pl.BlockSpec((s, LANES), lambda hp, g: (0, hp))
    f32buf = pltpu.VMEM((s, LANES), F32)
    return _pallas(
        body, name=name, grid=(n_pairs, len(DILATIONS)), in_specs=[qk_spec, qk_spec, v_spec],
        out_specs=[o_spec, o_spec],
        out_shape=[jax.ShapeDtypeStruct((s, n_pairs * LANES), F32)] * 2,
        operands=[qn, kn, proj],
        scratch_shapes=[pltpu.VMEM((s, LANES), BF16), pltpu.VMEM((s + 2 * HALF_SPAN, LANES), BF16),
                        pltpu.VMEM((s + 2 * HALF_SPAN, 2 * LANES), BF16)] + [f32buf] * 9
        + [pltpu.VMEM((QBLK, KWIN), F32), pltpu.VMEM((2, 2 * QBLK, KWIN), F32),
           pltpu.VMEM((2, 2 * QBLK, LANES), F32)],
        rider=rider)


def _attn_bwd(qn, kn, proj, dattn, attn, lse, bd, name, rider=None):
    s = qn.shape[0]
    n_pairs = N_SLOT_HEADS * HEAD_DIM // LANES
    v_col0 = 2 * qn.shape[1] // LANES
    nt_dims = (((1,), (1,)), ((), ()))
    tn_dims = (((0,), (0,)), ((), ()))
    spad = s + 2 * HALF_SPAN

    def body(q_ref, k_ref, v_ref, do_ref, o_ref, lse_ref, bd_ref, dq_ref, dk_ref, dv_ref,
             q_rm, k_rm, v_rm, do_rm, lse0_rm, lse1_rm, dd0_rm, dd1_rm, dq_rm, dk_rm, dv_rm,
             lse0_p, lse1_p, dd0_p, dd1_p, band, p_buf, ds_buf):
        g = pl.program_id(1)
        zpad = jnp.zeros((HALF_SPAN, LANES), BF16)
        for buf in (k_rm, v_rm):
            buf[0:HALF_SPAN, :] = zpad
            buf[s + HALF_SPAN: spad, :] = zpad
        zf = jnp.zeros((HALF_SPAN, LANES), F32)
        for buf in (dk_rm, dv_rm):
            buf[0:HALF_SPAN, :] = zf
            buf[s + HALF_SPAN: spad, :] = zf
        band[...] = _band_bias()

        def clear(rows):
            z = jnp.zeros((rows.size, LANES), F32)
            dk_rm[pl.ds(rows.start + HALF_SPAN, rows.size), :] = z
            dv_rm[pl.ds(rows.start + HALF_SPAN, rows.size), :] = z

        _row_chunks(s, clear)

        def prepare(rows):
            lo = lax.broadcasted_iota(jnp.int32, (rows.size, LANES), 1) < HEAD_DIM
            dsum = _head_sums(do_ref[rows, :] * o_ref[rows, :], bd_ref[...])
            dswap = pltpu.roll(dsum, HEAD_DIM, axis=1)
            dd0_p[rows, :] = jnp.where(lo, dsum, dswap)
            dd1_p[rows, :] = jnp.where(lo, dswap, dsum)
            lv = lse_ref[rows, :]
            lswap = pltpu.roll(lv, HEAD_DIM, axis=1)
            lse0_p[rows, :] = jnp.where(lo, lv, lswap)
            lse1_p[rows, :] = jnp.where(lo, lswap, lv)

        @pl.when(g == 0)
        def _():
            _row_chunks(s, prepare)
        lane = lax.broadcasted_iota(jnp.int32, (QBLK, LANES), 1)
        low = lane < HEAD_DIM
        n_blk = s // QBLK

        def stacked(ref, rows):
            val = ref[rows, :]
            zero = jnp.zeros_like(val)
            return jnp.concatenate([jnp.where(low, val, zero), jnp.where(low, zero, val)], axis=0)

        for gi, d in enumerate(DILATIONS):
            @pl.when(g == gi)
            def _(d=d):
                seq = s // d
                _to_residue_major(q_rm, q_ref, s, d, cast=BF16)
                _to_residue_major(k_rm, k_ref, s, d, dst_off=HALF_SPAN, cast=BF16)
                _to_residue_major(v_rm, v_ref, s, d, dst_off=HALF_SPAN, cast=BF16)
                _to_residue_major(do_rm, do_ref, s, d, cast=BF16)
                for dst_, src_ in ((lse0_rm, lse0_p), (lse1_rm, lse1_p), (dd0_rm, dd0_p), (dd1_rm, dd1_p)):
                    _to_residue_major(dst_, src_, s, d)

                def scores(b, slot):
                    base = _block_base(b)
                    rows = pl.ds(base, QBLK)
                    win = pl.ds(base, KWIN)
                    sc = lax.dot_general(stacked(q_rm, rows), k_rm[win, :], nt_dims, preferred_element_type=F32)
                    dp = lax.dot_general(stacked(do_rm, rows), v_rm[win, :], nt_dims, preferred_element_type=F32)
                    bias = band[...] + _range_bias(base, seq)
                    for hh, (lse_r, dd_r) in enumerate(((lse0_rm, dd0_rm), (lse1_rm, dd1_rm))):
                        r = slice(hh * QBLK, (hh + 1) * QBLK)
                        lse_h = lse_r[rows, :]
                        dd_h = dd_r[rows, :]
                        sh = sc[r, :] + bias
                        p = jnp.exp(jnp.concatenate([sh[:, 0:LANES] - lse_h, sh[:, LANES:KWIN] - lse_h], axis=1))
                        dph = dp[r, :]
                        ds = p * jnp.concatenate([dph[:, 0:LANES] - dd_h, dph[:, LANES:KWIN] - dd_h], axis=1)
                        p_buf[slot, r, :] = p.astype(BF16)
                        ds_buf[slot, r, :] = ds.astype(BF16)

                def grads(b, slot):
                    base = _block_base(b)
                    rows = pl.ds(base, QBLK)
                    win = pl.ds(base, KWIN)
                    p = p_buf[slot]
                    ds = ds_buf[slot]
                    dq2 = jnp.dot(ds, k_rm[win, :], preferred_element_type=F32)
                    dq_rm[rows, :] = jnp.where(low, dq2[0:QBLK, :], dq2[QBLK:2 * QBLK, :])
                    dk_rm[win, :] += lax.dot_general(ds, stacked(q_rm, rows), tn_dims, preferred_element_type=F32)
                    dv_rm[win, :] += lax.dot_general(p, stacked(do_rm, rows), tn_dims, preferred_element_type=F32)

                scores(0, 0)

                def pair(i, carry):
                    b = 2 * i
                    grads(b, 0)
                    scores(b + 1, 1)
                    grads(b + 1, 1)
                    scores(b + 2, 0)
                    return carry

                lax.fori_loop(0, n_blk // 2 - 1, pair, 0)
                grads(n_blk - 2, 0)
                scores(n_blk - 1, 1)
                grads(n_blk - 1, 1)
                _from_residue_major(dq_ref, dq_rm, s, d)
                _from_residue_major(dk_ref, dk_rm, s, d, src_off=HALF_SPAN)
                _from_residue_major(dv_ref, dv_rm, s, d, src_off=HALF_SPAN)

    qk_spec = pl.BlockSpec((s, LANES), lambda hp, g: (0, g * n_pairs + hp))
    v_spec = pl.BlockSpec((s, LANES), lambda hp, g: (0, v_col0 + g * n_pairs + hp))
    o_spec = pl.BlockSpec((s, LANES), lambda hp, g: (0, hp))
    width = qn.shape[1]
    f32buf = pltpu.VMEM((s, LANES), F32)
    f32pad = pltpu.VMEM((spad, LANES), F32)
    return _pallas(
        body, name=name, grid=(n_pairs, len(DILATIONS)),
        in_specs=[qk_spec, qk_spec, v_spec, o_spec, o_spec, o_spec,
                  pl.BlockSpec((LANES, LANES), lambda hp, g: (0, 0))],
        out_specs=[qk_spec, qk_spec, qk_spec],
        out_shape=[jax.ShapeDtypeStruct((s, width), F32)] * 3,
        operands=[qn, kn, proj, dattn, attn, lse, bd],
        scratch_shapes=[pltpu.VMEM((s, LANES), BF16), pltpu.VMEM((spad, LANES), BF16),
                        pltpu.VMEM((spad, LANES), BF16), pltpu.VMEM((s, LANES), BF16),
                        f32buf, f32buf, f32buf, f32buf, f32buf, f32pad, f32pad,
                        f32buf, f32buf, f32buf, f32buf, pltpu.VMEM((QBLK, KWIN), F32),
                        pltpu.VMEM((2, 2 * QBLK, KWIN), BF16), pltpu.VMEM((2, 2 * QBLK, KWIN), BF16)],
        rider=rider)


CONV_PAD = 16


def _conv_fwd(proj, conv_w, conv_b, col0, name, rider=None):
    s = proj.shape[0]
    ch = conv_w.shape[1]
    nblk = ch // LANES
    a0 = col0 // LANES
    tr = 256
    shift = CONV_PAD - (CONV_WIDTH - 1) // 2

    def body(a_ref, b_ref, w_ref, bias_ref, u0_ref, uc_ref, pad):
        z = jnp.zeros((CONV_PAD, LANES), F32)
        pad[0:CONV_PAD, :] = z
        pad[s + CONV_PAD: s + 2 * CONV_PAD, :] = z

        def glu(rows):
            u0 = a_ref[rows, :] * jax.nn.sigmoid(b_ref[rows, :])
            u0_ref[rows, :] = u0
            pad[pl.ds(rows.start + CONV_PAD, rows.size), :] = u0

        _row_chunks(s, glu)
        for t in range(0, s, tr):
            acc = jnp.broadcast_to(bias_ref[...], (tr, LANES))
            for k in range(CONV_WIDTH):
                acc = acc + w_ref[k:k + 1, :] * pad[t + k + shift: t + k + shift + tr, :]
            uc_ref[t:t + tr, :] = acc

    return _pallas(
        body, name=name, grid=(nblk,),
        in_specs=[pl.BlockSpec((s, LANES), lambda c: (0, a0 + c)),
                  pl.BlockSpec((s, LANES), lambda c: (0, a0 + nblk + c)),
                  pl.BlockSpec((CONV_WIDTH, LANES), lambda c: (0, c)),
                  pl.BlockSpec((1, LANES), lambda c: (0, c))],
        out_specs=[pl.BlockSpec((s, LANES), lambda c: (0, c))] * 2,
        out_shape=[jax.ShapeDtypeStruct((s, ch), F32)] * 2, operands=[proj, proj, conv_w, conv_b],
        scratch_shapes=[pltpu.VMEM((s + 2 * CONV_PAD, LANES), F32)], rider=rider)


def _ln_silu_fwd(uc, ln_w, ln_b, name):
    s, ch = uc.shape
    tm = 256

    def body(u_ref, w_ref, b_ref, o_ref):
        u = u_ref[...]
        mu = jnp.mean(u, axis=-1, keepdims=True)
        xc = u - mu
        rstd = lax.rsqrt(jnp.mean(xc * xc, axis=-1, keepdims=True) + EPS)
        z = xc * rstd * w_ref[...] + b_ref[...]
        o_ref[...] = (z * jax.nn.sigmoid(z)).astype(BF16)

    row = pl.BlockSpec((tm, ch), lambda i: (i, 0))
    vec = pl.BlockSpec((1, ch), lambda i: (0, 0))
    return pl.pallas_call(
        body, name=name, grid=(s // tm,), in_specs=[row, vec, vec], out_specs=row,
        out_shape=jax.ShapeDtypeStruct((s, ch), BF16), compiler_params=_params(),
    )(uc, ln_w, ln_b)


def _ln_silu_bwd(du3, uc, ln_w, ln_b, name):
    s, ch = uc.shape
    tm = 256

    def body(d_ref, u_ref, w_ref, b_ref, du_ref, dw_ref, db_ref):
        u = u_ref[...]
        mu = jnp.mean(u, axis=-1, keepdims=True)
        xc = u - mu
        rstd = lax.rsqrt(jnp.mean(xc * xc, axis=-1, keepdims=True) + EPS)
        xhat = xc * rstd
        z = xhat * w_ref[...] + b_ref[...]
        sg = jax.nn.sigmoid(z)
        dz = d_ref[...] * (sg * (1.0 + z * (1.0 - sg)))
        dxh = dz * w_ref[...]
        du_ref[...] = rstd * (dxh - jnp.mean(dxh, axis=-1, keepdims=True)
                              - xhat * jnp.mean(dxh * xhat, axis=-1, keepdims=True))
        pw = jnp.sum(dz * xhat, axis=0, keepdims=True)
        pb = jnp.sum(dz, axis=0, keepdims=True)
        first = pl.program_id(0) == 0

        @pl.when(first)
        def _():
            dw_ref[...] = pw
            db_ref[...] = pb

        @pl.when(jnp.logical_not(first))
        def _():
            dw_ref[...] += pw
            db_ref[...] += pb

    row = pl.BlockSpec((tm, ch), lambda i: (i, 0))
    vec = pl.BlockSpec((1, ch), lambda i: (0, 0))
    return pl.pallas_call(
        body, name=name, grid=(s // tm,), in_specs=[row, row, vec, vec], out_specs=[row, vec, vec],
        out_shape=[jax.ShapeDtypeStruct((s, ch), F32), jax.ShapeDtypeStruct((1, ch), F32),
                   jax.ShapeDtypeStruct((1, ch), F32)],
        compiler_params=_params(),
    )(du3, uc, ln_w, ln_b)


def _conv_bwd(duc, u0, proj, conv_w, col0, name, rider=None):
    s = proj.shape[0]
    ch = conv_w.shape[1]
    nblk = ch // LANES
    a0 = col0 // LANES
    tr = 256
    half = (CONV_WIDTH - 1) // 2
    shift = CONV_PAD - half

    def body(duc_ref, u0_ref, a_ref, b_ref, w_ref, da_ref, db_ref, dw_ref, dbias_ref, pad_d, pad_u):
        z = jnp.zeros((CONV_PAD, LANES), F32)
        for buf in (pad_d, pad_u):
            buf[0:CONV_PAD, :] = z
            buf[s + CONV_PAD: s + 2 * CONV_PAD, :] = z

        def fill(rows):
            dst = pl.ds(rows.start + CONV_PAD, rows.size)
            pad_d[dst, :] = duc_ref[rows, :]
            pad_u[dst, :] = u0_ref[rows, :]

        _row_chunks(s, fill)
        dw_acc = [jnp.zeros((8, LANES), F32) for _ in range(CONV_WIDTH)]
        dbias_acc = jnp.zeros((8, LANES), F32)
        for t in range(0, s, tr):
            d_t = duc_ref[t:t + tr, :]
            dbias_acc = dbias_acc + jnp.sum(d_t.reshape(tr // 8, 8, LANES), axis=0)
            du0 = jnp.zeros((tr, LANES), F32)
            for k in range(CONV_WIDTH):
                du0 = du0 + w_ref[k:k + 1, :] * pad_d[t - k + half + CONV_PAD: t - k + half + CONV_PAD + tr, :]
                prod = d_t * pad_u[t + k + shift: t + k + shift + tr, :]
                dw_acc[k] = dw_acc[k] + jnp.sum(prod.reshape(tr // 8, 8, LANES), axis=0)
            av = a_ref[t:t + tr, :]
            sg = jax.nn.sigmoid(b_ref[t:t + tr, :])
            da_ref[t:t + tr, :] = (du0 * sg).astype(BF16)
            db_ref[t:t + tr, :] = (du0 * av * sg * (1.0 - sg)).astype(BF16)
        for k in range(CONV_WIDTH):
            dw_ref[k:k + 1, :] = jnp.sum(dw_acc[k], axis=0, keepdims=True)
        dbias_ref[...] = jnp.sum(dbias_acc, axis=0, keepdims=True)

    col = lambda off: pl.BlockSpec((s, LANES), lambda c: (0, off + c))
    return _pallas(
        body, name=name, grid=(nblk,),
        in_specs=[col(0), col(0), col(a0), col(a0 + nblk),
                  pl.BlockSpec((CONV_WIDTH, LANES), lambda c: (0, c))],
        out_specs=[col(0), col(0), pl.BlockSpec((CONV_WIDTH, LANES), lambda c: (0, c)),
                   pl.BlockSpec((1, LANES), lambda c: (0, c))],
        out_shape=[jax.ShapeDtypeStruct((s, ch), BF16)] * 2
        + [jax.ShapeDtypeStruct((CONV_WIDTH, ch), F32), jax.ShapeDtypeStruct((1, ch), F32)],
        operands=[duc, u0, proj, proj, conv_w],
        scratch_shapes=[pltpu.VMEM((s + 2 * CONV_PAD, LANES), F32)] * 2, rider=rider)


GATE_BLK = 512


def _gate_fwd(proj, bg, y_a, y_b, col0, name):
    s, d = y_a.shape
    tm = 256
    g0 = col0 // GATE_BLK
    nb = d // GATE_BLK

    def body(ga_ref, gb_ref, ba_ref, bb_ref, ya_ref, yb_ref, o_ref):
        g_a = jax.nn.sigmoid(ga_ref[...] + ba_ref[...])
        g_b = jax.nn.sigmoid(gb_ref[...] + bb_ref[...])
        o_ref[...] = (g_a * ya_ref[...] + g_b * yb_ref[...]).astype(BF16)

    act = pl.BlockSpec((tm, GATE_BLK), lambda i, j: (i, j))
    return pl.pallas_call(
        body, name=name, grid=(s // tm, nb),
        in_specs=[pl.BlockSpec((tm, GATE_BLK), lambda i, j: (i, g0 + j)),
                  pl.BlockSpec((tm, GATE_BLK), lambda i, j: (i, g0 + nb + j)),
                  pl.BlockSpec((None, 1, GATE_BLK), lambda i, j: (0, 0, j)),
                  pl.BlockSpec((None, 1, GATE_BLK), lambda i, j: (1, 0, j)), act, act],
        out_specs=act, out_shape=jax.ShapeDtypeStruct((s, d), BF16), compiler_params=_params(),
    )(proj, proj, bg, bg, y_a, y_b)


def _gate_bwd(d_mixed, proj, bg, y_a, y_b, col0, name, rider=None):
    s, d = y_a.shape
    tm = 256
    g0 = col0 // GATE_BLK
    nb = d // GATE_BLK

    def body(dm_ref, gl_ref, b_ref, y_ref, dgl_ref, dy_ref, db_ref):
        gate = jax.nn.sigmoid(gl_ref[...] + b_ref[...])
        dm = dm_ref[...]
        dy_ref[...] = (dm * gate).astype(BF16)
        dgl = dm * y_ref[...] * gate * (1.0 - gate)
        dgl_ref[...] = dgl.astype(BF16)
        part = jnp.sum(dgl, axis=0, keepdims=True)
        first = pl.program_id(1) == 0

        @pl.when(first)
        def _():
            db_ref[...] = part

        @pl.when(jnp.logical_not(first))
        def _():
            db_ref[...] += part

    def call(y, branch, dproj_prev, rider):
        def wrapped(*refs):
            if dproj_prev is not None:
                refs = refs[1:]
            body(*refs)

        ins = [pl.BlockSpec((tm, GATE_BLK), lambda j, i: (i, j)),
               pl.BlockSpec((tm, GATE_BLK), lambda j, i: (i, g0 + branch * nb + j)),
               pl.BlockSpec((None, 1, GATE_BLK), lambda j, i: (branch, 0, j)),
               pl.BlockSpec((tm, GATE_BLK), lambda j, i: (i, j))]
        ops = [d_mixed, proj, bg, y]
        alias = {}
        if dproj_prev is not None:
            ins = [ANY] + ins
            ops = [dproj_prev] + ops
            alias = {0: 0}
        return _pallas(
            wrapped, name=f"{name}_{branch}", grid=(nb, s // tm), in_specs=ins,
            out_specs=[pl.BlockSpec((tm, GATE_BLK), lambda j, i: (i, g0 + branch * nb + j)),
                       pl.BlockSpec((tm, GATE_BLK), lambda j, i: (i, j)),
                       pl.BlockSpec((1, GATE_BLK), lambda j, i: (0, j))],
            out_shape=[jax.ShapeDtypeStruct((s, proj.shape[1]), BF16), jax.ShapeDtypeStruct((s, d), BF16),
                       jax.ShapeDtypeStruct((1, d), F32)],
            operands=ops, aliases=alias, rider=rider)

    first = call(y_a, 0, None, rider)
    (dproj, dy_a, db_a), rider_out = first if rider is not None else (first, None)
    dproj, dy_b, db_b = call(y_b, 1, dproj, None)
    res = (dproj, dy_a, dy_b, db_a, db_b)
    return res if rider is None else (res, rider_out)


def _ffn_in_swiglu(h2, w_blocked, name):
    s, k = h2.shape
    nblk, _, tn = w_blocked.shape
    ff = nblk // 2 * tn
    tm = 512

    def body(a_ref, wg_ref, wu_ref, g_ref, u_ref, act_ref):
        a = a_ref[...]
        gt = jnp.dot(a, wg_ref[...], preferred_element_type=F32)
        up = jnp.dot(a, wu_ref[...], preferred_element_type=F32)
        g_ref[...] = gt
        u_ref[...] = up
        act_ref[...] = (gt * jax.nn.sigmoid(gt) * up).astype(BF16)

    out = pl.BlockSpec((tm, tn), lambda i, j: (i, j))
    return pl.pallas_call(
        body, name=name, grid=(s // tm, nblk // 2),
        in_specs=[pl.BlockSpec((tm, k), lambda i, j: (i, 0)),
                  pl.BlockSpec((None, k, tn), lambda i, j: (j, 0, 0)),
                  pl.BlockSpec((None, k, tn), lambda i, j: (nblk // 2 + j, 0, 0))],
        out_specs=[out, out, out],
        out_shape=[jax.ShapeDtypeStruct((s, ff), F32), jax.ShapeDtypeStruct((s, ff), F32),
                   jax.ShapeDtypeStruct((s, ff), BF16)],
        compiler_params=_params(),
    )(h2, w_blocked, w_blocked)


def _swiglu_bwd(gate, up, d_act, name, rider=None):
    s, ff = gate.shape
    tm = 256

    def body(g_ref, u_ref, d_ref, o_ref):
        gt = g_ref[...]
        sg = jax.nn.sigmoid(gt)
        dv = d_ref[...]
        o_ref[:, 0:ff] = (dv * u_ref[...] * (sg * (1.0 + gt * (1.0 - sg)))).astype(BF16)
        o_ref[:, ff:2 * ff] = (dv * gt * sg).astype(BF16)

    row = pl.BlockSpec((tm, ff), lambda i: (i, 0))
    return _pallas(
        body, name=name, grid=(s // tm,), in_specs=[row, row, row],
        out_specs=pl.BlockSpec((tm, 2 * ff), lambda i: (i, 0)),
        out_shape=jax.ShapeDtypeStruct((s, 2 * ff), BF16), operands=[gate, up, d_act], rider=rider)


def _loss_fwd_bwd(y, target, name):
    s, d = y.shape
    tm = 256

    def body(y_ref, t_ref, dy_ref, dyb_ref, loss_ref, acc):
        diff = y_ref[...] - t_ref[...]
        dy = diff * (1.0 / d)
        dy_ref[...] = dy
        dyb_ref[...] = dy.astype(BF16)
        part = jnp.sum((diff * diff).reshape(tm // 8, 8, d), axis=0)
        i = pl.program_id(0)

        @pl.when(i == 0)
        def _():
            acc[...] = part

        @pl.when(i > 0)
        def _():
            acc[...] += part

        @pl.when(i == pl.num_programs(0) - 1)
        def _():
            loss_ref[...] = (0.5 / d) * jnp.sum(jnp.sum(acc[...], axis=1, keepdims=True), axis=0, keepdims=True)

    row = pl.BlockSpec((tm, d), lambda i: (i, 0))
    return pl.pallas_call(
        body, name=name, grid=(s // tm,), in_specs=[row, row],
        out_specs=[row, row, pl.BlockSpec((1, 1), lambda i: (0, 0))],
        out_shape=[jax.ShapeDtypeStruct((s, d), F32), jax.ShapeDtypeStruct((s, d), BF16),
                   jax.ShapeDtypeStruct((1, 1), F32)],
        scratch_shapes=[pltpu.VMEM((8, d), F32)], compiler_params=_params(),
    )(y, target)


LATE_GATHER = ("w_o_attn", "w_pw_conv", "w_out", "w_ffn_in", "w_ffn_out")
EARLY_REDUCE = LATE_GATHER


def _blocks_by_half(g):
    if g.ndim == 2:
        g = g.reshape(N_CHIPS, g.shape[0] // N_CHIPS, g.shape[1])
    return g.reshape(N_CHIPS, 2, g.shape[1] // 2, g.shape[2])


def _forward_backward(x, pos_col, target, wts, late_bufs, pos_arr):
    wts = dict(wts)
    consts = _rope_consts()
    bd = consts[3]
    qw2 = jnp.tile(wts["q_norm_w"], (1, LANES // HEAD_DIM))
    kw2 = jnp.tile(wts["k_norm_w"], (1, LANES // HEAD_DIM))
    qkv_w = 3 * N_SLOT_HEADS * HEAD_DIM
    conv_col0 = 3 * qkv_w
    ch = wts["conv_w"].shape[1]
    gate_col0 = conv_col0 + 2 * ch

    h = _rmsnorm_fwd(x, wts["norm1_w"], "rms1_fwd")
    late = dict(zip(LATE_GATHER, late_bufs))
    quarter = late["w_ffn_in"].shape[2] // 4
    proj, (late["w_o_attn"], late["w_pw_conv"], late["w_out"], late["w_ffn_in"]) = _matmul(
        h, wts["w_in"], mode="nn", tm=512, tn=1920, tk=1024, out_dtype=F32, name="mm_proj", b_blocked=True,
        rider=_gather_ici_rider([late["w_o_attn"], late["w_pw_conv"], late["w_out"], late["w_ffn_in"]], [],
                                row_ranges=[None, None, None, (0, quarter)]))
    (qn, kn), (late["w_ffn_in"],) = _qk_fwd(
        proj, pos_col, qw2, kw2, consts, "qk_fwd",
        rider=_gather_ici_rider([late["w_ffn_in"]], [], row_ranges=[(quarter, 2 * quarter)]))
    (attn, lse), (late["w_ffn_in"], late["w_ffn_out"]) = _attn_fwd(
        qn, kn, proj, "attn_fwd",
        rider=_gather_ici_rider([late["w_ffn_in"], late["w_ffn_out"]], [],
                                row_ranges=[(3 * quarter, quarter), None]))
    (u0, uc), late_bufs = _conv_fwd(proj, wts["conv_w"], wts["conv_b"], conv_col0, "conv_fwd",
                                    rider=_gather_forward_rider([late[n] for n in LATE_GATHER]))
    for n, buf in zip(LATE_GATHER, late_bufs):
        full = buf.reshape(N_CHIPS, -1, buf.shape[3])
        wts[n] = full.reshape(-1, full.shape[2]) if n in ROW_SHARDED else full
    attn_b = attn.astype(BF16)
    y_a = _matmul(attn_b, wts["w_o_attn"], mode="nn", tm=1024, tn=256, tk=512, out_dtype=F32, name="mm_ya",
                  b_blocked=True)
    u3 = _ln_silu_fwd(uc, wts["conv_ln_w"], wts["conv_ln_b"], "ln_fwd")
    y_b = _matmul(u3, wts["w_pw_conv"], mode="nn", tm=1024, tn=256, tk=512, out_dtype=F32, name="mm_yb",
                  b_blocked=True)
    mixed = _gate_fwd(proj, wts["b_gate"], y_a, y_b, gate_col0, "gate_fwd")
    x1 = _matmul(mixed, wts["w_out"], mode="nn", tm=512, tn=1024, tk=1024, out_dtype=F32, name="mm_x1",
                 residual=x)
    h2 = _rmsnorm_fwd(x1, wts["norm2_w"], "rms2_fwd")
    gate, up, act = _ffn_in_swiglu(h2, wts["w_ffn_in"], "mm_gu_swiglu")
    x2 = _matmul(act, wts["w_ffn_out"], mode="nn", tm=512, tn=1024, tk=2816, out_dtype=F32, name="mm_x2",
                 residual=x1)
    dy, dy_b16, loss = _loss_fwd_bwd(x2, target, "loss")

    g = {}
    by_chip = {}

    def pair_add(n, blocks, received):
        return _add_own_half(blocks, received, pos_arr, f"grads_pair_add_{n}")

    d_act = _matmul(dy_b16, wts["w_ffn_out"], mode="nt", tm=512, tn=1408, tk=1024, out_dtype=F32, name="mm_dact")
    g_ffn_out = _blocks_by_half(
        _matmul(act, dy_b16, mode="tn", tm=1408, tn=1024, tk=2048, out_dtype=F32, name="mm_dwffnout"))
    dgu, (received,) = _swiglu_bwd(gate, up, d_act, "swiglu_bwd",
                                   rider=_pair_exchange_rider([g_ffn_out], halved=True))
    to_send, own = pair_add("w_ffn_out", g_ffn_out, received)
    dh2, (by_chip["w_ffn_out"],) = _matmul(
        dgu, wts["w_ffn_in"], mode="nt", tm=1024, tn=1024, tk=1408, out_dtype=F32, name="mm_dh2", b_blocked=True,
        rider=_chip_exchange_rider([to_send], [own]))
    g_ffn_in = _blocks_by_half(_matmul(h2, dgu, mode="tn", tm=512, tn=1408, tk=2048, out_dtype=F32,
                                       name="mm_dwffnin", out_blocked=N_CHIPS))
    dx1, dx1_b16, g["norm2_w"] = _rmsnorm_bwd(dh2, x1, wts["norm2_w"], dy, "rms2_bwd")
    d_mixed = _matmul(dx1_b16, wts["w_out"], mode="nt", tm=512, tn=1024, tk=1024, out_dtype=F32, name="mm_dmixed")
    g["w_out"] = _matmul(mixed, dx1_b16, mode="tn", tm=512, tn=1024, tk=2048, out_dtype=F32, name="mm_dwout")
    (dproj, dy_a, dy_b, db_a, db_b), (received,) = _gate_bwd(
        d_mixed, proj, wts["b_gate"], y_a, y_b, gate_col0, "gate_bwd",
        rider=_pair_exchange_rider([g_ffn_in], halved=True))
    ffn_in_to_send, ffn_in_own = pair_add("w_ffn_in", g_ffn_in, received)
    g["b_gate"] = jnp.concatenate([db_a, db_b], axis=0)
    dattn = _matmul(dy_a, wts["w_o_attn"], mode="nt", tm=1024, tn=512, tk=256, out_dtype=F32, name="mm_dattn",
                    b_blocked=True)
    g["w_o_attn"] = _matmul(attn_b, dy_a, mode="tn", tm=512, tn=256, tk=2048, out_dtype=F32, name="mm_dwo",
                            out_blocked=N_CHIPS)
    du3 = _matmul(dy_b, wts["w_pw_conv"], mode="nt", tm=1024, tn=512, tk=256, out_dtype=F32, name="mm_du3",
                  b_blocked=True)
    g["w_pw_conv"] = _matmul(u3, dy_b, mode="tn", tm=512, tn=256, tk=2048, out_dtype=F32, name="mm_dwpw",
                             out_blocked=N_CHIPS)
    duc, g["conv_ln_w"], g["conv_ln_b"] = _ln_silu_bwd(du3, uc, wts["conv_ln_w"], wts["conv_ln_b"], "ln_bwd")

    small3 = ("w_out", "w_o_attn", "w_pw_conv")
    g_small3 = [_blocks_by_half(g.pop(n)) for n in small3]
    (da, db, g["conv_w"], g["conv_b"]), received = _conv_bwd(
        duc, u0, proj, wts["conv_w"], conv_col0, "conv_bwd", rider=_pair_exchange_rider(g_small3, halved=True))
    sums3 = [pair_add(n, gb, rv) for n, gb, rv in zip(small3, g_small3, received)]
    (dqn, dkn, dv), (by_chip["w_ffn_in"],) = _attn_bwd(
        qn, kn, proj, dattn, attn, lse, bd, "attn_bwd",
        rider=_chip_exchange_rider([ffn_in_to_send], [ffn_in_own]))
    (dproj, dqw, dkw), exchanged3 = _qk_bwd(
        dproj, dqn, dkn, dv, da, db, proj, pos_col, qw2, kw2, consts, "qk_bwd",
        rider=_chip_exchange_rider([s[0] for s in sums3], [s[1] for s in sums3]))
    by_chip.update(zip(small3, exchanged3))
    halves = [_sum_chips(by_chip[n], pos_arr, f"grads_chip_sum_{n}") for n in EARLY_REDUCE]
    g["q_norm_w"] = dqw[:, :HEAD_DIM]
    g["k_norm_w"] = dkw[:, :HEAD_DIM]

    c = pos_arr[0]
    rh = h.shape[1] // 2
    h_sibling = lax.dynamic_slice_in_dim(h, (1 - c) * rh, rh, axis=1)
    h_own = lax.dynamic_slice_in_dim(h, c * rh, rh, axis=1)
    g_sibling, shards = _matmul(h_sibling, dproj, mode="tn", tm=rh, tn=1920, tk=2048, out_dtype=F32,
                                name="mm_dwin_sibling", out_blocked=N_CHIPS, rider=_pair_gather_rider(halves))
    reduced = dict(zip(EARLY_REDUCE, shards))
    g_own, from_sibling = _matmul(h_own, dproj, mode="tn", tm=rh, tn=1920, tk=2048, out_dtype=F32,
                                  name="mm_dwin_own", out_blocked=N_CHIPS,
                                  rider=_pair_exchange_rider([g_sibling], halved=False))
    to_send, own = _add_own_half(g_own, from_sibling[0], pos_arr, "grads_pair_add_w_in")
    first_rows = 288
    dh, by_chip_w_in = _matmul(dproj, wts["w_in"], mode="nt", tm=1024, tn=1024, tk=1920, out_dtype=F32, name="mm_dh",
                               b_blocked=True,
                               rider=_chip_exchange_rider([to_send], [own], row_range=(0, first_rows)))
    (grad_x, _, g["norm1_w"]), by_chip_w_in = _rmsnorm_bwd(
        dh, x, wts["norm1_w"], dx1, "rms1_bwd",
        rider=_chip_exchange_rider([to_send], by_chip_w_in, row_range=(first_rows, rh - first_rows)))
    return loss, grad_x, g, reduced, by_chip_w_in[0]


def _mesh_pos():
    return lax.axis_index("x"), lax.axis_index("y"), lax.axis_index("c")


def _other_chips(x, y):
    return [(1 - x, y), (x, 1 - y), (1 - x, 1 - y)]


def _cast_into_slot(shard, chip_arr, dtype, name):
    r, c = shard.shape
    tr = r // 2 if r % 32 == 0 else r

    def body(chip_ref, s_ref, o_ref):
        del chip_ref
        o_ref[...] = s_ref[...].astype(dtype)

    return pl.pallas_call(
        body, name=name,
        grid_spec=pltpu.PrefetchScalarGridSpec(
            num_scalar_prefetch=1, grid=(r // tr,),
            in_specs=[pl.BlockSpec((tr, c), lambda i, chip_ref: (i, 0))],
            out_specs=pl.BlockSpec((None, tr, c), lambda i, chip_ref: (chip_ref[0], i, 0))),
        out_shape=jax.ShapeDtypeStruct((N_CHIPS, r, c), dtype), compiler_params=_params(),
    )(chip_arr, shard)


def _allgather_inplace(big, small, name):
    nb, ns = len(big), len(small)
    n = nb + ns

    def body(*refs):
        bufs = refs[n:2 * n]
        send_sems, recv_sems, fsend_sems, frecv_sems = refs[2 * n:]
        x, y, c = _mesh_pos()
        me = 2 * x + y
        chips = _other_chips(x, y)

        def part(a, slot, half):
            return bufs[a].at[slot, half] if a < nb else bufs[a].at[slot]

        sends = []
        for a in range(n):
            for k, (px, py) in enumerate(chips):
                cp = pltpu.make_async_remote_copy(
                    src_ref=part(a, me, c), dst_ref=part(a, me, c), send_sem=send_sems.at[a, k],
                    recv_sem=recv_sems.at[a, k], device_id=(px, py, c), device_id_type=MESH)
                cp.start()
                sends.append(cp)
        for a in range(n):
            for k, (px, py) in enumerate(chips):
                slot = 2 * px + py
                pltpu.make_async_remote_copy(
                    src_ref=part(a, slot, c), dst_ref=part(a, slot, c), send_sem=send_sems.at[a, k],
                    recv_sem=recv_sems.at[a, k], device_id=(px, py, c), device_id_type=MESH).wait_recv()
                if a < nb:
                    fwd = pltpu.make_async_remote_copy(
                        src_ref=part(a, slot, c), dst_ref=part(a, slot, c), send_sem=fsend_sems.at[a, k],
                        recv_sem=frecv_sems.at[a, k], device_id=(x, y, 1 - c), device_id_type=MESH)
                    fwd.start()
                    sends.append(fwd)
        for a in range(nb):
            for k, (px, py) in enumerate(chips):
                slot = 2 * px + py
                pltpu.make_async_remote_copy(
                    src_ref=part(a, slot, 1 - c), dst_ref=part(a, slot, 1 - c), send_sem=fsend_sems.at[a, k],
                    recv_sem=frecv_sems.at[a, k], device_id=(x, y, 1 - c), device_id_type=MESH).wait_recv()
        for cp in sends:
            cp.wait_send()

    ops = list(big) + list(small)
    return pl.pallas_call(
        body, name=name, in_specs=[ANY] * n, out_specs=[ANY] * n,
        out_shape=[jax.ShapeDtypeStruct(o.shape, o.dtype) for o in ops],
        input_output_aliases={i: i for i in range(n)},
        scratch_shapes=[pltpu.SemaphoreType.DMA((n, 3)), pltpu.SemaphoreType.DMA((n, 3)),
                        pltpu.SemaphoreType.DMA((nb, 3)), pltpu.SemaphoreType.DMA((nb, 3))],
    )(*ops)


def _comm_call(rider, name):
    def body():
        pass

    return _pallas(body, name=name, grid=(1,), in_specs=[], out_specs=[], out_shape=[], operands=[],
                   rider=rider)[1]


def _gather_ici_rider(big, small, row_ranges=None):
    nb = len(big)
    n = nb + len(small)
    row_ranges = row_ranges or [None] * nb

    def copies(bufs, sems):
        x, y, c = _mesh_pos()
        me = 2 * x + y

        def part(a, slot):
            if a >= nb:
                return bufs[a].at[slot]
            if row_ranges[a] is None:
                return bufs[a].at[slot, c]
            return bufs[a].at[slot, c, pl.ds(*row_ranges[a])]
        out = []
        for a in range(n):
            for k, (px, py) in enumerate(_other_chips(x, y)):
                send = functools.partial(
                    pltpu.make_async_remote_copy,
                    src_ref=part(a, me), dst_ref=part(a, me), send_sem=sems[0].at[a, k],
                    recv_sem=sems[1].at[a, k], device_id=(px, py, c), device_id_type=MESH)
                recv = functools.partial(
                    pltpu.make_async_remote_copy,
                    src_ref=part(a, 2 * px + py), dst_ref=part(a, 2 * px + py), send_sem=sems[0].at[a, k],
                    recv_sem=sems[1].at[a, k], device_id=(px, py, c), device_id_type=MESH)
                out.append((send, recv))
        return out

    def start(r_in, r_out, sems):
        for send, _ in copies(r_out, sems):
            send().start()

    def wait(r_in, r_out, sems):
        cps = copies(r_out, sems)
        for _, recv in cps:
            recv().wait_recv()
        for send, _ in cps:
            send().wait_send()

    ops = list(big) + list(small)
    return _Rider(ops, [jax.ShapeDtypeStruct(o.shape, o.dtype) for o in ops], {i: i for i in range(n)},
                  [pltpu.SemaphoreType.DMA((n, 3)), pltpu.SemaphoreType.DMA((n, 3))], start, wait)


def _gather_forward_rider(big):
    n = len(big)

    def copies(bufs, sems):
        x, y, c = _mesh_pos()
        out = []
        for a in range(n):
            for k, (px, py) in enumerate(_other_chips(x, y)):
                slot = 2 * px + py
                send = functools.partial(
                    pltpu.make_async_remote_copy,
                    src_ref=bufs[a].at[slot, c], dst_ref=bufs[a].at[slot, c], send_sem=sems[0].at[a, k],
                    recv_sem=sems[1].at[a, k], device_id=(x, y, 1 - c), device_id_type=MESH)
                recv = functools.partial(
                    pltpu.make_async_remote_copy,
                    src_ref=bufs[a].at[slot, 1 - c], dst_ref=bufs[a].at[slot, 1 - c], send_sem=sems[0].at[a, k],
                    recv_sem=sems[1].at[a, k], device_id=(x, y, 1 - c), device_id_type=MESH)
                out.append((send, recv))
        return out

    def start(r_in, r_out, sems):
        for send, _ in copies(r_out, sems):
            send().start()

    def wait(r_in, r_out, sems):
        cps = copies(r_out, sems)
        for _, recv in cps:
            recv().wait_recv()
        for send, _ in cps:
            send().wait_send()

    return _Rider(big, [jax.ShapeDtypeStruct(o.shape, o.dtype) for o in big], {i: i for i in range(n)},
                  [pltpu.SemaphoreType.DMA((n, 3)), pltpu.SemaphoreType.DMA((n, 3))], start, wait)


def _pair_exchange_rider(gs, halved):
    n = len(gs)

    def copies(r_in, r_out, sems):
        x, y, c = _mesh_pos()
        return [pltpu.make_async_remote_copy(
            src_ref=r_in[a].at[:, 1 - c] if halved else r_in[a], dst_ref=r_out[a], send_sem=sems[0].at[a],
            recv_sem=sems[1].at[a], device_id=(x, y, 1 - c), device_id_type=MESH) for a in range(n)]

    def start(r_in, r_out, sems):
        for cp in copies(r_in, r_out, sems):
            cp.start()

    def wait(r_in, r_out, sems):
        for cp in copies(r_in, r_out, sems):
            cp.wait()

    return _Rider(gs, [jax.ShapeDtypeStruct((g.shape[0],) + g.shape[-2:], g.dtype) for g in gs], {},
                  [pltpu.SemaphoreType.DMA((n,)), pltpu.SemaphoreType.DMA((n,))], start, wait)


def _chip_exchange_rider(to_send, by_chip, row_range=None):
    n = len(to_send)

    def copies(r_in, r_out, sems):
        x, y, c = _mesh_pos()
        me = 2 * x + y
        rows = (lambda ref: ref) if row_range is None else (lambda ref: ref.at[pl.ds(*row_range)])
        out = []
        for a in range(n):
            for k, (px, py) in enumerate(_other_chips(x, y)):
                send = functools.partial(
                    pltpu.make_async_remote_copy,
                    src_ref=rows(r_in[a].at[2 * px + py]), dst_ref=rows(r_out[a].at[me]),
                    send_sem=sems[0].at[a, k], recv_sem=sems[1].at[a, k], device_id=(px, py, c),
                    device_id_type=MESH)
                recv = functools.partial(
                    pltpu.make_async_remote_copy,
                    src_ref=rows(r_in[a].at[me]), dst_ref=rows(r_out[a].at[2 * px + py]),
                    send_sem=sems[0].at[a, k], recv_sem=sems[1].at[a, k], device_id=(px, py, c),
                    device_id_type=MESH)
                out.append((send, recv))
        return out

    def start(r_in, r_out, sems):
        for send, _ in copies(r_in, r_out, sems):
            send().start()

    def wait(r_in, r_out, sems):
        cps = copies(r_in, r_out, sems)
        for _, recv in cps:
            recv().wait_recv()
        for send, _ in cps:
            send().wait_send()

    return _Rider(list(to_send) + list(by_chip), [jax.ShapeDtypeStruct(b.shape, b.dtype) for b in by_chip],
                  {n + i: i for i in range(n)},
                  [pltpu.SemaphoreType.DMA((n, 3)), pltpu.SemaphoreType.DMA((n, 3))], start, wait)


def _pair_gather_rider(bufs):
    n = len(bufs)

    def copies(r_out, sems):
        x, y, c = _mesh_pos()
        out = []
        for a in range(n):
            send = functools.partial(
                    pltpu.make_async_remote_copy,
                src_ref=r_out[a].at[c], dst_ref=r_out[a].at[c], send_sem=sems[0].at[a],
                recv_sem=sems[1].at[a], device_id=(x, y, 1 - c), device_id_type=MESH)
            recv = functools.partial(
                    pltpu.make_async_remote_copy,
                src_ref=r_out[a].at[1 - c], dst_ref=r_out[a].at[1 - c], send_sem=sems[0].at[a],
                recv_sem=sems[1].at[a], device_id=(x, y, 1 - c), device_id_type=MESH)
            out.append((send, recv))
        return out

    def start(r_in, r_out, sems):
        for send, _ in copies(r_out, sems):
            send().start()

    def wait(r_in, r_out, sems):
        cps = copies(r_out, sems)
        for _, recv in cps:
            recv().wait_recv()
        for send, _ in cps:
            send().wait_send()

    return _Rider(bufs, [jax.ShapeDtypeStruct(b.shape, b.dtype) for b in bufs], {i: i for i in range(n)},
                  [pltpu.SemaphoreType.DMA((n,)), pltpu.SemaphoreType.DMA((n,))], start, wait)


def _add_own_half(g, recv, pos_arr, name):
    nb, rh, cols = g.shape[0], g.shape[-2], g.shape[-1]

    def body(pos_ref, g_ref, r_ref, send_ref, own_ref):
        s = (g_ref[...] + r_ref[...]).astype(BF16)
        send_ref[...] = s

        @pl.when(pl.program_id(0) == pos_ref[1])
        def _():
            own_ref[...] = s

    blk = pl.BlockSpec((None, rh, cols), lambda j, pos_ref: (j, 0, 0))
    g_spec = blk if g.ndim == 3 else pl.BlockSpec((None, None, rh, cols),
                                                   lambda j, pos_ref: (j, pos_ref[0], 0, 0))
    shape = jax.ShapeDtypeStruct((nb, rh, cols), BF16)
    return pl.pallas_call(
        body, name=name,
        grid_spec=pltpu.PrefetchScalarGridSpec(
            num_scalar_prefetch=1, grid=(nb,), in_specs=[g_spec, blk],
            out_specs=[blk, pl.BlockSpec((None, rh, cols), lambda j, pos_ref: (pos_ref[1], 0, 0))]),
        out_shape=[shape, shape], compiler_params=_params(),
    )(pos_arr, g, recv)


def _sum_chips(gath, pos_arr, name):
    nb, rh, cols = gath.shape

    def body(pos_ref, a_ref, b_ref, c_ref, d_ref, o_ref):
        del pos_ref
        o_ref[...] = ((a_ref[...].astype(F32) + b_ref[...].astype(F32)) + c_ref[...].astype(F32)) \
            + d_ref[...].astype(F32)

    tr = rh // 2 if (rh // 2) % 16 == 0 else rh
    specs = [pl.BlockSpec((None, tr, cols), functools.partial(lambda i, pos_ref, j: (j, i, 0), j=j))
             for j in range(nb)]
    return pl.pallas_call(
        body, name=name,
        grid_spec=pltpu.PrefetchScalarGridSpec(
            num_scalar_prefetch=1, grid=(rh // tr,), in_specs=specs,
            out_specs=pl.BlockSpec((None, tr, cols), lambda i, pos_ref: (pos_ref[0], i, 0))),
        out_shape=jax.ShapeDtypeStruct((2, rh, cols), F32), compiler_params=_params(),
    )(pos_arr, gath, gath, gath, gath)


def _small_allreduce(v, name, rider=None):
    n = v.shape[0]
    n_dev = 8

    def body(v_ref, o_ref, buf, send_sems, recv_sems):
        x, y, c = _mesh_pos()
        me = 4 * x + 2 * y + c
        buf[me] = v_ref[...]
        sends = []
        peers = []
        for r in range(1, n_dev):
            px = 1 - x if r & 4 else x
            py = 1 - y if r & 2 else y
            pc = 1 - c if r & 1 else c
            peers.append((px, py, pc))
            cp = pltpu.make_async_remote_copy(
                src_ref=v_ref, dst_ref=buf.at[me], send_sem=send_sems.at[r - 1],
                recv_sem=recv_sems.at[r - 1], device_id=(px, py, pc), device_id_type=MESH)
            cp.start()
            sends.append(cp)
        for r, (px, py, pc) in enumerate(peers):
            pltpu.make_async_remote_copy(
                src_ref=v_ref, dst_ref=buf.at[4 * px + 2 * py + pc], send_sem=send_sems.at[r],
                recv_sem=recv_sems.at[r], device_id=(px, py, pc), device_id_type=MESH).wait_recv()
        for cp in sends:
            cp.wait_send()
        acc = buf[0]
        for i in range(1, n_dev):
            acc = acc + buf[i]
        o_ref[...] = acc

    whole = pl.BlockSpec(v.shape, lambda i: (0, 0))
    return _pallas(
        body, name=name, grid=(1,), in_specs=[whole], out_specs=whole,
        out_shape=jax.ShapeDtypeStruct(v.shape, v.dtype), operands=[v],
        scratch_shapes=[pltpu.VMEM((n_dev, n, LANES), F32), pltpu.SemaphoreType.DMA((n_dev - 1,)),
                        pltpu.SemaphoreType.DMA((n_dev - 1,))],
        rider=rider)


def _adamw_math(w, g, m, v):
    m = ADAM_B1 * m + (1.0 - ADAM_B1) * g
    v = ADAM_B2 * v + (1.0 - ADAM_B2) * (g * g)
    m_hat = m / (1.0 - ADAM_B1 ** ADAM_STEP)
    v_hat = v / (1.0 - ADAM_B2 ** ADAM_STEP)
    delta = -ADAM_LR * (m_hat / (jnp.sqrt(v_hat) + ADAM_EPS) + ADAM_WD * w)
    return delta, m, v


def _adamw(w, g, m, v, name):
    r, c = w.shape
    tr = 128 if r % 128 == 0 else 64
    assert r % tr == 0

    def body(w_ref, g_ref, m_ref, v_ref, go_ref, d_ref, mo_ref, vo_ref):
        gv = g_ref[...]
        d, mn, vn = _adamw_math(w_ref[...], gv, m_ref[...], v_ref[...])
        go_ref[...] = gv
        d_ref[...] = d
        mo_ref[...] = mn
        vo_ref[...] = vn

    blk = pl.BlockSpec((tr, c), lambda i: (i, 0))
    return _pallas(body, name=name, grid=(r // tr,), in_specs=[blk] * 4, out_specs=[blk] * 4,
                   out_shape=[jax.ShapeDtypeStruct((r, c), F32)] * 4, operands=[w, g, m, v])


def _adamw_small(ws, gs, ms, vs, name):
    n = len(ws)

    def body(*refs):
        w_r, g_r, m_r, v_r = refs[:n], refs[n:2 * n], refs[2 * n:3 * n], refs[3 * n:4 * n]
        d_o, m_o, v_o = refs[4 * n:5 * n], refs[5 * n:6 * n], refs[6 * n:7 * n]
        for i in range(n):
            d, mn, vn = _adamw_math(w_r[i][...], g_r[i][...], m_r[i][...], v_r[i][...])
            d_o[i][...] = d
            m_o[i][...] = mn
            v_o[i][...] = vn

    vm = pl.BlockSpec(memory_space=pltpu.VMEM)
    shapes = [jax.ShapeDtypeStruct(w.shape, F32) for w in ws]
    outs = pl.pallas_call(
        body, name=name, in_specs=[vm] * (4 * n), out_specs=[vm] * (3 * n), out_shape=shapes * 3,
        compiler_params=_params(),
    )(*ws, *gs, *ms, *vs)
    return outs[:n], outs[n:2 * n], outs[2 * n:]


BIG = ("w_in", "w_o_attn", "w_pw_conv", "w_out", "w_ffn_in", "w_ffn_out")
ROW_SHARDED = ("w_out", "w_ffn_out")
SMALL = ("norm1_w", "b_gate", "q_norm_w", "k_norm_w", "conv_w", "conv_b", "conv_ln_w", "conv_ln_b", "norm2_w")
ORDER = ("norm1_w", "w_in", "b_gate", "q_norm_w", "k_norm_w", "w_o_attn", "conv_w", "conv_b", "conv_ln_w",
         "conv_ln_b", "w_pw_conv", "w_out", "norm2_w", "w_ffn_in", "w_ffn_out")
PACK_TILE = 8 * LANES


def _pack_small(parts):
    rows = []
    for p in parts:
        flat = p.reshape(-1)
        pad = (-flat.shape[0]) % PACK_TILE
        rows.append(jnp.pad(flat, (0, pad)).reshape(-1, LANES))
    return jnp.concatenate(rows, axis=0)


def _unpack_small(packed, shapes):
    out, row = [], 0
    for shp in shapes:
        size = int(np.prod(shp))
        nrow = -(-size // PACK_TILE) * (PACK_TILE // LANES)
        out.append(packed[row:row + nrow].reshape(-1)[:size].reshape(shp))
        row += nrow
    return out


def kernel(x, positions, norm1_w, w_in, b_gate, q_norm_w, k_norm_w, w_o_attn, conv_w, conv_b, conv_ln_w, conv_ln_b, w_pw_conv, w_out, norm2_w, w_ffn_in, w_ffn_out, loss_target, m_norm1_w, m_w_in, m_b_gate, m_q_norm_w, m_k_norm_w, m_w_o_attn, m_conv_w, m_conv_b, m_conv_ln_w, m_conv_ln_b, m_w_pw_conv, m_w_out, m_norm2_w, m_w_ffn_in, m_w_ffn_out, v_norm1_w, v_w_in, v_b_gate, v_q_norm_w, v_k_norm_w, v_w_o_attn, v_conv_w, v_conv_b, v_conv_ln_w, v_conv_ln_b, v_w_pw_conv, v_w_out, v_norm2_w, v_w_ffn_in, v_w_ffn_out):
    w = dict(norm1_w=norm1_w, w_in=w_in, b_gate=b_gate, q_norm_w=q_norm_w, k_norm_w=k_norm_w, w_o_attn=w_o_attn,
             conv_w=conv_w, conv_b=conv_b, conv_ln_w=conv_ln_w, conv_ln_b=conv_ln_b, w_pw_conv=w_pw_conv,
             w_out=w_out, norm2_w=norm2_w, w_ffn_in=w_ffn_in, w_ffn_out=w_ffn_out)
    m = dict(norm1_w=m_norm1_w, w_in=m_w_in, b_gate=m_b_gate, q_norm_w=m_q_norm_w, k_norm_w=m_k_norm_w,
             w_o_attn=m_w_o_attn, conv_w=m_conv_w, conv_b=m_conv_b, conv_ln_w=m_conv_ln_w,
             conv_ln_b=m_conv_ln_b, w_pw_conv=m_w_pw_conv, w_out=m_w_out, norm2_w=m_norm2_w,
             w_ffn_in=m_w_ffn_in, w_ffn_out=m_w_ffn_out)
    v = dict(norm1_w=v_norm1_w, w_in=v_w_in, b_gate=v_b_gate, q_norm_w=v_q_norm_w, k_norm_w=v_k_norm_w,
             w_o_attn=v_w_o_attn, conv_w=v_conv_w, conv_b=v_conv_b, conv_ln_w=v_conv_ln_w,
             conv_ln_b=v_conv_ln_b, w_pw_conv=v_w_pw_conv, w_out=v_w_out, norm2_w=v_norm2_w,
             w_ffn_in=v_w_ffn_in, w_ffn_out=v_w_ffn_out)
    cx, cy, cc = _mesh_pos()
    chip = 2 * cx + cy

    chip_arr = chip.reshape(1).astype(jnp.int32)
    pos_arr = jnp.stack([cc, chip]).astype(jnp.int32)
    bufs = {}
    for n in BIG:
        buf = _cast_into_slot(w[n][0], chip_arr, BF16, f"cast_{n}")
        bufs[n] = buf.reshape(N_CHIPS, 2, buf.shape[1] // 2, buf.shape[2])
    small_bufs = [_cast_into_slot(w[n][0], chip_arr, F32, f"slot_{n}") for n in ("conv_w", "b_gate")]
    w_in_buf, conv_w_buf, b_gate_buf = _allgather_inplace([bufs["w_in"]], small_bufs, "allgather_w_in")
    wts = dict(w_in=w_in_buf.reshape(N_CHIPS, -1, w_in_buf.shape[3]),
               conv_w=conv_w_buf.transpose(1, 0, 2).reshape(CONV_WIDTH, -1),
               b_gate=b_gate_buf.transpose(1, 0, 2).reshape(2, 1, -1),
               norm1_w=norm1_w, q_norm_w=q_norm_w, k_norm_w=k_norm_w, conv_b=conv_b, conv_ln_w=conv_ln_w,
               conv_ln_b=conv_ln_b, norm2_w=norm2_w)

    loss, grad_x, g, reduced, by_chip_w_in = _forward_backward(
        x[0], positions.reshape(-1, 1), loss_target[0], wts, [bufs[n] for n in LATE_GATHER], pos_arr)
    grads = {n: b.reshape(-1, b.shape[2]) for n, b in reduced.items()}
    half_w_in = _sum_chips(by_chip_w_in, pos_arr, "grads_chip_sum_w_in")

    small_parts = [loss] + [g[n] for n in SMALL]
    small_shapes = [p.shape for p in small_parts]
    summed, (shard_w_in,) = _small_allreduce(_pack_small(small_parts), "small_allreduce",
                                             rider=_pair_gather_rider([half_w_in]))
    grads["w_in"] = shard_w_in.reshape(-1, shard_w_in.shape[2])
    reduced = _unpack_small(summed, small_shapes)
    loss_total = reduced[0].reshape(())
    for n, r in zip(SMALL, reduced[1:]):
        grads[n] = r
    ch_shard = conv_w.shape[2]
    grads["conv_w"] = lax.dynamic_slice_in_dim(grads["conv_w"], chip * ch_shard, ch_shard, axis=1)
    d_shard = b_gate.shape[2]
    grads["b_gate"] = lax.dynamic_slice_in_dim(grads["b_gate"], chip * d_shard, d_shard, axis=1)

    delta, new_m, new_v = {}, {}, {}
    for n in BIG:
        grads[n], delta[n], new_m[n], new_v[n] = _adamw(w[n][0], grads[n], m[n][0], v[n][0], f"adamw_{n}")
    flat2 = lambda a: a.reshape(-1, a.shape[-1])
    d_s, m_s, v_s = _adamw_small([flat2(w[n]) for n in SMALL], [flat2(grads[n]) for n in SMALL],
                                 [flat2(m[n]) for n in SMALL], [flat2(v[n]) for n in SMALL], "adamw_small")
    for i, n in enumerate(SMALL):
        delta[n], new_m[n], new_v[n] = d_s[i], m_s[i], v_s[i]

    shaped = lambda d, n: d[n].reshape(w[n].shape)
    return (loss_total, grad_x[None], *[shaped(grads, n) for n in ORDER], *[shaped(delta, n) for n in ORDER],
            *[shaped(new_m, n) for n in ORDER], *[shaped(new_v, n) for n in ORDER])
```

```python
import functools

import numpy as np
import jax
import jax.numpy as jnp
from jax import lax
from jax.experimental import pallas as pl
from jax.experimental.pallas import tpu as pltpu

F32 = jnp.float32
BF16 = jnp.bfloat16
MESH = pl.DeviceIdType.MESH
ANY = pl.BlockSpec(memory_space=pl.ANY)

HEAD_DIM = 64
N_SLOT_HEADS = 8
DILATIONS = (1, 4, 16)
HALF_SPAN = 64
ROPE_THETA = 500000.0
ROT_DIM = 16
CONV_WIDTH = 31
EPS = 1e-6
NEG_INF = -1e30
ADAM_LR, ADAM_B1, ADAM_B2, ADAM_EPS, ADAM_WD, ADAM_STEP = 0.001, 0.9, 0.999, 1e-08, 0.01, 10

LANES = 128
QBLK = 128
KWIN = QBLK + 2 * HALF_SPAN
VMEM_LIMIT = 48 * 1024 * 1024
N_CHIPS = 4


def _params(**kw):
    return pltpu.CompilerParams(vmem_limit_bytes=VMEM_LIMIT, **kw)


class _Rider:
    def __init__(self, operands, out_shapes, aliases, scratch, start, wait):
        self.operands, self.out_shapes, self.aliases = list(operands), list(out_shapes), dict(aliases)
        self.scratch, self.start, self.wait = list(scratch), start, wait


def _pallas(body, *, name, grid, in_specs, out_specs, out_shape, operands, scratch_shapes=(), aliases=None,
            rider=None):
    single = not isinstance(out_specs, (list, tuple))
    out_specs_l = [out_specs] if single else list(out_specs)
    out_shape_l = [out_shape] if single else list(out_shape)
    aliases = dict(aliases or {})
    if rider is None:
        res = pl.pallas_call(
            body, name=name, grid=grid, in_specs=list(in_specs), out_specs=out_specs_l, out_shape=out_shape_l,
            scratch_shapes=list(scratch_shapes), input_output_aliases=aliases, compiler_params=_params(),
        )(*operands)
        return res[0] if single else res
    n_in, n_rin = len(in_specs), len(rider.operands)
    n_out, n_rout = len(out_specs_l), len(rider.out_shapes)
    n_sc = len(scratch_shapes)

    def wrapped(*refs):
        main_in, r_in = refs[:n_in], refs[n_in:n_in + n_rin]
        o0 = n_in + n_rin
        main_out, r_out = refs[o0:o0 + n_out], refs[o0 + n_out:o0 + n_out + n_rout]
        s0 = o0 + n_out + n_rout
        main_sc, r_sc = refs[s0:s0 + n_sc], refs[s0 + n_sc:]
        ids = [pl.program_id(d) for d in range(len(grid))]
        first = functools.reduce(jnp.logical_and, [i == 0 for i in ids])
        last = functools.reduce(jnp.logical_and, [i == n - 1 for i, n in zip(ids, grid)])

        @pl.when(first)
        def _():
            rider.start(r_in, r_out, r_sc)

        body(*main_in, *main_out, *main_sc)

        @pl.when(last)
        def _():
            rider.wait(r_in, r_out, r_sc)

    for src, dst in rider.aliases.items():
        aliases[n_in + src] = n_out + dst
    res = pl.pallas_call(
        wrapped, name=name, grid=grid, in_specs=list(in_specs) + [ANY] * n_rin,
        out_specs=out_specs_l + [ANY] * n_rout, out_shape=out_shape_l + rider.out_shapes,
        scratch_shapes=list(scratch_shapes) + rider.scratch, input_output_aliases=aliases,
        compiler_params=_params(),
    )(*operands, *rider.operands)
    main = res[:n_out]
    return (main[0] if single else main), res[n_out:]


def _matmul(a, b, *, mode, tm, tn, tk, out_dtype, name, b_blocked=False,
            out_blocked=None, residual=None, rider=None):
    a_shape = a.shape
    if mode == "nn":
        m_dim, k_dim = a_shape
        n_dim = b.shape[0] * b.shape[2] if b_blocked else b.shape[1]
        rows, cols, red = m_dim, n_dim, k_dim
    elif mode == "nt":
        m_dim, n_dim = a_shape
        k_dim = b.shape[1] if b_blocked else b.shape[0]
        rows, cols, red = m_dim, k_dim, n_dim
    else:
        m_dim, k_dim = a_shape
        n_dim = b.shape[1]
        rows, cols, red = k_dim, n_dim, m_dim
    assert rows % tm == 0 and cols % tn == 0 and red % tk == 0, (name, rows, cols, red)
    ni, nj, nk = rows // tm, cols // tn, red // tk

    if mode == "nn":
        a_spec = pl.BlockSpec((tm, tk), lambda i, j, k: (i, k))
        if b_blocked:
            per = b.shape[2] // tn
            b_spec = pl.BlockSpec((None, tk, tn), lambda i, j, k: (j // per, k, j % per))
        else:
            b_spec = pl.BlockSpec((tk, tn), lambda i, j, k: (k, j))
        dims = (((1,), (0,)), ((), ()))
    elif mode == "nt":
        a_spec = pl.BlockSpec((tm, tk), lambda i, j, k: (i, k))
        if b_blocked:
            per = b.shape[2] // tk
            b_spec = pl.BlockSpec((None, tn, tk), lambda i, j, k: (k // per, j, k % per))
        else:
            b_spec = pl.BlockSpec((tn, tk), lambda i, j, k: (j, k))
        dims = (((1,), (1,)), ((), ()))
    else:
        a_spec = pl.BlockSpec((tk, tm), lambda i, j, k: (k, i))
        b_spec = pl.BlockSpec((tk, tn), lambda i, j, k: (k, j))
        dims = (((0,), (0,)), ((), ()))

    if out_blocked:
        per_o = (cols // out_blocked) // tn
        out_spec = pl.BlockSpec((None, tm, tn), lambda i, j, k: (j // per_o, i, j % per_o))
        out_shape = jax.ShapeDtypeStruct((out_blocked, rows, cols // out_blocked), out_dtype)
    else:
        out_spec = pl.BlockSpec((tm, tn), lambda i, j, k: (i, j))
        out_shape = jax.ShapeDtypeStruct((rows, cols), out_dtype)

    in_specs = [a_spec, b_spec]
    operands = [a, b]
    if residual is not None:
        in_specs.append(pl.BlockSpec((tm, tn), lambda i, j, k: (i, j)))
        operands.append(residual)
    has_res = residual is not None

    def body(*refs):
        a_ref, b_ref = refs[0], refs[1]
        res_ref = refs[2] if has_res else None
        o_ref = refs[3] if has_res else refs[2]
        prod = lax.dot_general(a_ref[...], b_ref[...], dims, preferred_element_type=F32)

        def finish(val):
            if has_res:
                val = val + res_ref[...]
            o_ref[...] = val.astype(out_dtype)

        if nk == 1:
            finish(prod)
        else:
            acc_ref = refs[-1]
            k = pl.program_id(2)

            @pl.when(k == 0)
            def _():
                acc_ref[...] = prod

            @pl.when(k > 0)
            def _():
                acc_ref[...] += prod

            @pl.when(k == nk - 1)
            def _():
                finish(acc_ref[...])

    scratch = [pltpu.VMEM((tm, tn), F32)] if nk > 1 else []
    return _pallas(body, name=name, grid=(ni, nj, nk), in_specs=in_specs, out_specs=out_spec,
                   out_shape=out_shape, operands=operands, scratch_shapes=scratch, rider=rider)


def _rmsnorm_fwd(x, w, name):
    s, d = x.shape
    tm = 256

    def body(x_ref, w_ref, o_ref):
        xv = x_ref[...]
        rstd = lax.rsqrt(jnp.mean(xv * xv, axis=-1, keepdims=True) + EPS)
        o_ref[...] = (xv * rstd * w_ref[...]).astype(BF16)

    return pl.pallas_call(
        body, name=name, grid=(s // tm,),
        in_specs=[pl.BlockSpec((tm, d), lambda i: (i, 0)), pl.BlockSpec((1, d), lambda i: (0, 0))],
        out_specs=pl.BlockSpec((tm, d), lambda i: (i, 0)),
        out_shape=jax.ShapeDtypeStruct((s, d), BF16), compiler_params=_params(),
    )(x, w)


def _rmsnorm_bwd(dh, x, w, dres, name, rider=None):
    s, d = x.shape
    tm = 256

    def body(dh_ref, x_ref, w_ref, dres_ref, dx_ref, dxb_ref, dw_ref):
        xv = x_ref[...]
        rstd = lax.rsqrt(jnp.mean(xv * xv, axis=-1, keepdims=True) + EPS)
        xhat = xv * rstd
        dhv = dh_ref[...]
        g = dhv * w_ref[...]
        dx = rstd * (g - xhat * jnp.mean(g * xhat, axis=-1, keepdims=True)) + dres_ref[...]
        dx_ref[...] = dx
        dxb_ref[...] = dx.astype(BF16)
        part = jnp.sum(dhv * xhat, axis=0, keepdims=True)

        @pl.when(pl.program_id(0) == 0)
        def _():
            dw_ref[...] = part

        @pl.when(pl.program_id(0) > 0)
        def _():
            dw_ref[...] += part

    row = pl.BlockSpec((tm, d), lambda i: (i, 0))
    vec = pl.BlockSpec((1, d), lambda i: (0, 0))
    return _pallas(
        body, name=name, grid=(s // tm,), in_specs=[row, row, vec, row], out_specs=[row, row, vec],
        out_shape=[jax.ShapeDtypeStruct((s, d), F32), jax.ShapeDtypeStruct((s, d), BF16),
                   jax.ShapeDtypeStruct((1, d), F32)],
        operands=[dh, x, w, dres], rider=rider)


def _rope_consts():
    lane = np.arange(LANES)
    in_head = lane % HEAD_DIM
    inv_freq = ROPE_THETA ** (-jnp.arange(0, ROT_DIM, 2, dtype=F32) / ROT_DIM)
    invf = jnp.where(jnp.asarray(in_head < ROT_DIM), jnp.tile(inv_freq, LANES // (ROT_DIM // 2)), 0.0)
    m_a = np.where(in_head < ROT_DIM // 2, -1.0, 0.0).astype(np.float32)
    m_b = np.where((in_head >= ROT_DIM // 2) & (in_head < ROT_DIM), 1.0, 0.0).astype(np.float32)
    block_diag = (lane[:, None] // HEAD_DIM == lane[None, :] // HEAD_DIM).astype(np.float32)
    return (invf.reshape(1, LANES).astype(F32), jnp.asarray(m_a).reshape(1, LANES),
            jnp.asarray(m_b).reshape(1, LANES), jnp.asarray(block_diag, dtype=BF16))


def _head_sums(v, bd):
    hi = v.astype(BF16)
    lo = (v - hi.astype(F32)).astype(BF16)
    return jnp.dot(hi, bd, preferred_element_type=F32) + jnp.dot(lo, bd, preferred_element_type=F32)


def _qk_fwd(proj, pos_col, qw2, kw2, consts, name, rider=None):
    s = proj.shape[0]
    width = 3 * N_SLOT_HEADS * HEAD_DIM
    tm = 128
    invf, m_a, m_b, bd = consts
    scale = HEAD_DIM ** -0.5

    def body(q_ref, k_ref, pos_ref, qw_ref, kw_ref, invf_ref, ma_ref, mb_ref, bd_ref, qo_ref, ko_ref):
        ang = pos_ref[...].astype(F32) * invf_ref[...]
        cos = jnp.cos(ang)
        sin = jnp.sin(ang)
        s_a = sin * ma_ref[...]
        s_b = sin * mb_ref[...]
        bdv = bd_ref[...]
        for src, w_ref, dst, sc in ((q_ref, qw_ref, qo_ref, scale), (k_ref, kw_ref, ko_ref, 1.0)):
            for cb in range(width // LANES):
                cols = slice(cb * LANES, (cb + 1) * LANES)
                t = src[:, cols]
                rstd = lax.rsqrt(_head_sums(t * t, bdv) * (1.0 / HEAD_DIM) + EPS)
                y = t * rstd * w_ref[...]
                r = y * cos + pltpu.roll(y, LANES - 8, axis=1) * s_a + pltpu.roll(y, 8, axis=1) * s_b
                dst[:, cols] = r * sc if sc != 1.0 else r

    vec = pl.BlockSpec((1, LANES), lambda i: (0, 0))
    return _pallas(
        body, name=name, grid=(s // tm,),
        in_specs=[pl.BlockSpec((tm, width), lambda i: (i, 0)), pl.BlockSpec((tm, width), lambda i: (i, 1)),
                  pl.BlockSpec((tm, 1), lambda i: (i, 0)), vec, vec, vec, vec, vec,
                  pl.BlockSpec((LANES, LANES), lambda i: (0, 0))],
        out_specs=[pl.BlockSpec((tm, width), lambda i: (i, 0))] * 2,
        out_shape=[jax.ShapeDtypeStruct((s, width), F32)] * 2,
        operands=[proj, proj, pos_col, qw2, kw2, invf, m_a, m_b, bd], rider=rider)


def _qk_bwd(dqn, dkn, dv, da, db, dgl, proj, pos_col, qw2, kw2, consts, name, rider=None):
    s = proj.shape[0]
    width = 3 * N_SLOT_HEADS * HEAD_DIM
    ch = da.shape[1]
    gate_w = dgl.shape[1]
    out_w = 3 * width + 2 * ch + gate_w
    assert out_w == proj.shape[1]
    tm = 128
    invf, m_a, m_b, bd = consts
    scale = HEAD_DIM ** -0.5

    def body(dq_ref, dk_ref, dv_ref, da_ref, db_ref, dgl_ref, q_ref, k_ref, pos_ref, qw_ref, kw_ref,
             invf_ref, ma_ref, mb_ref, bd_ref, out_ref, dqw_ref, dkw_ref):
        ang = pos_ref[...].astype(F32) * invf_ref[...]
        cos = jnp.cos(ang)
        sin = jnp.sin(ang)
        s_a = sin * ma_ref[...]
        s_b = sin * mb_ref[...]
        bdv = bd_ref[...]
        first = pl.program_id(0) == 0
        for src, dsrc, w_ref, col0, dw_ref, sc in ((q_ref, dq_ref, qw_ref, 0, dqw_ref, scale),
                                                   (k_ref, dk_ref, kw_ref, width, dkw_ref, 1.0)):
            dw_acc = jnp.zeros((1, LANES), F32)
            for cb in range(width // LANES):
                cols = slice(cb * LANES, (cb + 1) * LANES)
                t = src[:, cols]
                dr = dsrc[:, cols]
                if sc != 1.0:
                    dr = dr * sc
                dy = dr * cos + pltpu.roll(dr * s_a, 8, axis=1) + pltpu.roll(dr * s_b, LANES - 8, axis=1)
                rstd = lax.rsqrt(_head_sums(t * t, bdv) * (1.0 / HEAD_DIM) + EPS)
                xhat = t * rstd
                g = dy * w_ref[...]
                dt = rstd * (g - xhat * (_head_sums(g * xhat, bdv) * (1.0 / HEAD_DIM)))
                out_ref[:, col0 + cb * LANES: col0 + (cb + 1) * LANES] = dt.astype(BF16)
                dw_acc = dw_acc + jnp.sum(dy * xhat, axis=0, keepdims=True)
            dw_acc = dw_acc + pltpu.roll(dw_acc, HEAD_DIM, axis=1)

            @pl.when(first)
            def _(dw_ref=dw_ref, dw_acc=dw_acc):
                dw_ref[...] = dw_acc

            @pl.when(jnp.logical_not(first))
            def _(dw_ref=dw_ref, dw_acc=dw_acc):
                dw_ref[...] += dw_acc
        out_ref[:, 2 * width: 3 * width] = dv_ref[...].astype(BF16)
        out_ref[:, 3 * width: 3 * width + ch] = da_ref[...]
        out_ref[:, 3 * width + ch: 3 * width + 2 * ch] = db_ref[...]
        out_ref[:, 3 * width + 2 * ch: out_w] = dgl_ref[...]

    vec = pl.BlockSpec((1, LANES), lambda i: (0, 0))
    blk = lambda c: pl.BlockSpec((tm, width), lambda i: (i, c))
    cblk = pl.BlockSpec((tm, ch), lambda i: (i, 0))
    return _pallas(
        body, name=name, grid=(s // tm,),
        in_specs=[blk(0), blk(0), blk(0), cblk, cblk, pl.BlockSpec((tm, gate_w), lambda i: (i, 0)),
                  blk(0), blk(1), pl.BlockSpec((tm, 1), lambda i: (i, 0)), vec, vec, vec, vec, vec,
                  pl.BlockSpec((LANES, LANES), lambda i: (0, 0))],
        out_specs=[pl.BlockSpec((tm, out_w), lambda i: (i, 0)), vec, vec],
        out_shape=[jax.ShapeDtypeStruct((s, out_w), BF16)] + [jax.ShapeDtypeStruct((1, LANES), F32)] * 2,
        operands=[dqn, dkn, dv, da, db, dgl, proj, proj, pos_col, qw2, kw2, invf, m_a, m_b, bd],
        rider=rider)


def _row_chunks(n_rows, fn, chunk=256):
    def step(i, c):
        fn(pl.ds(pl.multiple_of(i * chunk, chunk), chunk))
        return c
    lax.fori_loop(0, n_rows // chunk, step, 0)


def _to_residue_major(dst, src, s, d, dst_off=0, cast=None):
    seq = s // d
    for r in range(d):
        v = src[...] if d == 1 else src[pl.ds(r, seq, stride=d), :]
        dst[dst_off + r * seq: dst_off + (r + 1) * seq, :] = v if cast is None else v.astype(cast)


def _from_residue_major(dst, src, s, d, src_off=0):
    seq = s // d
    for r in range(d):
        v = src[src_off + r * seq: src_off + (r + 1) * seq, :]
        if d == 1:
            dst[...] = v
        else:
            dst[pl.ds(r, seq, stride=d), :] = v


def _band_bias():
    qi = lax.broadcasted_iota(jnp.int32, (QBLK, KWIN), 0)
    kj = lax.broadcasted_iota(jnp.int32, (QBLK, KWIN), 1)
    return jnp.where(jnp.abs(kj - HALF_SPAN - qi) <= HALF_SPAN, 0.0, NEG_INF).astype(F32)


def _range_bias(base, seq):
    kj = lax.broadcasted_iota(jnp.int32, (1, KWIN), 1)
    lo = (base & -seq) - base + HALF_SPAN
    return jnp.where((kj >= lo) & (kj < lo + seq), 0.0, NEG_INF).astype(F32)


def _block_base(b):
    return b * QBLK if isinstance(b, int) else pl.multiple_of(b * QBLK, QBLK)


def _attn_fwd(qn, kn, proj, name, rider=None):
    s = qn.shape[0]
    n_pairs = N_SLOT_HEADS * HEAD_DIM // LANES
    v_col0 = 2 * qn.shape[1] // LANES
    nt_dims = (((1,), (1,)), ((), ()))

    def body(q_ref, k_ref, v_ref, attn_ref, lse_ref, q_rm, k_rm, v_rm, acc_rm, m_rm, l_rm,
             acc_p, m_p, l_p, m_run, l_run, acc_run, band, s_buf, m_buf):
        g = pl.program_id(1)
        zpad = jnp.zeros((HALF_SPAN, LANES), BF16)
        k_rm[0:HALF_SPAN, :] = zpad
        k_rm[s + HALF_SPAN: s + 2 * HALF_SPAN, :] = zpad
        v_rm[0:HALF_SPAN, 0:LANES] = zpad
        v_rm[s + HALF_SPAN: s + 2 * HALF_SPAN, 0:LANES] = zpad

        def ones_rows(rows):
            v_rm[pl.ds(rows.start, rows.size), LANES:2 * LANES] = jnp.ones((rows.size, LANES), BF16)

        _row_chunks(s + 2 * HALF_SPAN, ones_rows, chunk=2 * HALF_SPAN)
        band[...] = _band_bias()
        lane = lax.broadcasted_iota(jnp.int32, (QBLK, LANES), 1)
        low = lane < HEAD_DIM
        n_blk = s // QBLK

        for gi, d in enumerate(DILATIONS):
            @pl.when(g == gi)
            def _(gi=gi, d=d):
                seq = s // d
                _to_residue_major(q_rm, q_ref, s, d, cast=BF16)
                _to_residue_major(k_rm, k_ref, s, d, dst_off=HALF_SPAN, cast=BF16)
                _to_residue_major(v_rm.at[:, 0:LANES], v_ref, s, d, dst_off=HALF_SPAN, cast=BF16)

                def scores(b, slot):
                    base = _block_base(b)
                    q = q_rm[pl.ds(base, QBLK), :]
                    zero = jnp.zeros_like(q)
                    q2 = jnp.concatenate([jnp.where(low, q, zero), jnp.where(low, zero, q)], axis=0)
                    sc = lax.dot_general(q2, k_rm[pl.ds(base, KWIN), :], nt_dims, preferred_element_type=F32)
                    bias = band[...] + _range_bias(base, seq)
                    for hh in range(2):
                        rows = slice(hh * QBLK, (hh + 1) * QBLK)
                        sh = sc[rows, :] + bias
                        s_buf[slot, rows, :] = sh
                        m_buf[slot, rows, :] = jnp.broadcast_to(jnp.max(sh, axis=-1, keepdims=True), (QBLK, LANES))

                def outputs(b, slot):
                    base = _block_base(b)
                    sv = s_buf[slot]
                    mb = m_buf[slot]
                    p = jnp.exp(jnp.concatenate([sv[:, 0:LANES] - mb, sv[:, LANES:2 * LANES] - mb], axis=1))
                    pv = jnp.dot(p.astype(BF16), v_rm[pl.ds(base, KWIN), :], preferred_element_type=F32)
                    rows = pl.ds(base, QBLK)
                    acc_rm[rows, :] = jnp.where(low, pv[0:QBLK, 0:LANES], pv[QBLK:2 * QBLK, 0:LANES])
                    l_rm[rows, :] = jnp.where(low, pv[0:QBLK, LANES:2 * LANES], pv[QBLK:2 * QBLK, LANES:2 * LANES])
                    m_rm[rows, :] = jnp.where(low, mb[0:QBLK, :], mb[QBLK:2 * QBLK, :])

                scores(0, 0)

                def pair(i, carry):
                    b = 2 * i
                    outputs(b, 0)
                    scores(b + 1, 1)
                    outputs(b + 1, 1)
                    scores(b + 2, 0)
                    return carry

                lax.fori_loop(0, n_blk // 2 - 1, pair, 0)
                outputs(n_blk - 2, 0)
                scores(n_blk - 1, 1)
                outputs(n_blk - 1, 1)
                if d == 1:
                    src = (acc_rm, m_rm, l_rm)
                else:
                    for dst_, src_ in ((acc_p, acc_rm), (m_p, m_rm), (l_p, l_rm)):
                        _from_residue_major(dst_, src_, s, d)
                    src = (acc_p, m_p, l_p)

                def combine(rows):
                    a_g, m_g, l_g = src[0][rows, :], src[1][rows, :], src[2][rows, :]
                    if gi == 0:
                        m_new, l_new, a_new = m_g, l_g, a_g
                    else:
                        m_old = m_run[rows, :]
                        m_new = jnp.maximum(m_old, m_g)
                        w_old = jnp.exp(m_old - m_new)
                        w_g = jnp.exp(m_g - m_new)
                        l_new = l_run[rows, :] * w_old + l_g * w_g
                        a_new = acc_run[rows, :] * w_old + a_g * w_g
                    if gi == len(DILATIONS) - 1:
                        attn_ref[rows, :] = a_new / l_new
                        lse_ref[rows, :] = m_new + jnp.log(l_new)
                    else:
                        m_run[rows, :] = m_new
                        l_run[rows, :] = l_new
                        acc_run[rows, :] = a_new

                _row_chunks(s, combine)

    qk_spec = pl.BlockSpec((s, LANES), lambda hp, g: (0, g * n_pairs + hp))
    v_spec = pl.BlockSpec((s, LANES), lambda hp, g: (0, v_col0 + g * n_pairs + hp))
    o_spec = pl.BlockSpec((s, LANES), lambda hp, g: (0, hp))
    f32buf = pltpu.VMEM((s, LANES), F32)
    return _pallas(
        body, name=name, grid=(n_pairs, len(DILATIONS)), in_specs=[qk_spec, qk_spec, v_spec],
        out_specs=[o_spec, o_spec],
        out_shape=[jax.ShapeDtypeStruct((s, n_pairs * LANES), F32)] * 2,
        operands=[qn, kn, proj],
        scratch_shapes=[pltpu.VMEM((s, LANES), BF16), pltpu.VMEM((s + 2 * HALF_SPAN, LANES), BF16),
                        pltpu.VMEM((s + 2 * HALF_SPAN, 2 * LANES), BF16)] + [f32buf] * 9
        + [pltpu.VMEM((QBLK, KWIN), F32), pltpu.VMEM((2, 2 * QBLK, KWIN), F32),
           pltpu.VMEM((2, 2 * QBLK, LANES), F32)],
        rider=rider)


def _attn_bwd(qn, kn, proj, dattn, attn, lse, bd, name, rider=None):
    s = qn.shape[0]
    n_pairs = N_SLOT_HEADS * HEAD_DIM // LANES
    v_col0 = 2 * qn.shape[1] // LANES
    nt_dims = (((1,), (1,)), ((), ()))
    tn_dims = (((0,), (0,)), ((), ()))
    spad = s + 2 * HALF_SPAN

    def body(q_ref, k_ref, v_ref, do_ref, o_ref, lse_ref, bd_ref, dq_ref, dk_ref, dv_ref,
             q_rm, k_rm, v_rm, do_rm, lse0_rm, lse1_rm, dd0_rm, dd1_rm, dq_rm, dk_rm, dv_rm,
             lse0_p, lse1_p, dd0_p, dd1_p, band, p_buf, ds_buf):
        g = pl.program_id(1)
        zpad = jnp.zeros((HALF_SPAN, LANES), BF16)
        for buf in (k_rm, v_rm):
            buf[0:HALF_SPAN, :] = zpad
            buf[s + HALF_SPAN: spad, :] = zpad
        zf = jnp.zeros((HALF_SPAN, LANES), F32)
        for buf in (dk_rm, dv_rm):
            buf[0:HALF_SPAN, :] = zf
            buf[s + HALF_SPAN: spad, :] = zf
        band[...] = _band_bias()

        def clear(rows):
            z = jnp.zeros((rows.size, LANES), F32)
            dk_rm[pl.ds(rows.start + HALF_SPAN, rows.size), :] = z
            dv_rm[pl.ds(rows.start + HALF_SPAN, rows.size), :] = z

        _row_chunks(s, clear)

        def prepare(rows):
            lo = lax.broadcasted_iota(jnp.int32, (rows.size, LANES), 1) < HEAD_DIM
            dsum = _head_sums(do_ref[rows, :] * o_ref[rows, :], bd_ref[...])
            dswap = pltpu.roll(dsum, HEAD_DIM, axis=1)
            dd0_p[rows, :] = jnp.where(lo, dsum, dswap)
            dd1_p[rows, :] = jnp.where(lo, dswap, dsum)
            lv = lse_ref[rows, :]
            lswap = pltpu.roll(lv, HEAD_DIM, axis=1)
            lse0_p[rows, :] = jnp.where(lo, lv, lswap)
            lse1_p[rows, :] = jnp.where(lo, lswap, lv)

        @pl.when(g == 0)
        def _():
            _row_chunks(s, prepare)
        lane = lax.broadcasted_iota(jnp.int32, (QBLK, LANES), 1)
        low = lane < HEAD_DIM
        n_blk = s // QBLK

        def stacked(ref, rows):
            val = ref[rows, :]
            zero = jnp.zeros_like(val)
            return jnp.concatenate([jnp.where(low, val, zero), jnp.where(low, zero, val)], axis=0)

        for gi, d in enumerate(DILATIONS):
            @pl.when(g == gi)
            def _(d=d):
                seq = s // d
                _to_residue_major(q_rm, q_ref, s, d, cast=BF16)
                _to_residue_major(k_rm, k_ref, s, d, dst_off=HALF_SPAN, cast=BF16)
                _to_residue_major(v_rm, v_ref, s, d, dst_off=HALF_SPAN, cast=BF16)
                _to_residue_major(do_rm, do_ref, s, d, cast=BF16)
                for dst_, src_ in ((lse0_rm, lse0_p), (lse1_rm, lse1_p), (dd0_rm, dd0_p), (dd1_rm, dd1_p)):
                    _to_residue_major(dst_, src_, s, d)

                def scores(b, slot):
                    base = _block_base(b)
                    rows = pl.ds(base, QBLK)
                    win = pl.ds(base, KWIN)
                    sc = lax.dot_general(stacked(q_rm, rows), k_rm[win, :], nt_dims, preferred_element_type=F32)
                    dp = lax.dot_general(stacked(do_rm, rows), v_rm[win, :], nt_dims, preferred_element_type=F32)
                    bias = band[...] + _range_bias(base, seq)
                    for hh, (lse_r, dd_r) in enumerate(((lse0_rm, dd0_rm), (lse1_rm, dd1_rm))):
                        r = slice(hh * QBLK, (hh + 1) * QBLK)
                        lse_h = lse_r[rows, :]
                        dd_h = dd_r[rows, :]
                        sh = sc[r, :] + bias
                        p = jnp.exp(jnp.concatenate([sh[:, 0:LANES] - lse_h, sh[:, LANES:KWIN] - lse_h], axis=1))
                        dph = dp[r, :]
                        ds = p * jnp.concatenate([dph[:, 0:LANES] - dd_h, dph[:, LANES:KWIN] - dd_h], axis=1)
                        p_buf[slot, r, :] = p.astype(BF16)
                        ds_buf[slot, r, :] = ds.astype(BF16)

                def grads(b, slot):
                    base = _block_base(b)
                    rows = pl.ds(base, QBLK)
                    win = pl.ds(base, KWIN)
                    p = p_buf[slot]
                    ds = ds_buf[slot]
                    dq2 = jnp.dot(ds, k_rm[win, :], preferred_element_type=F32)
                    dq_rm[rows, :] = jnp.where(low, dq2[0:QBLK, :], dq2[QBLK:2 * QBLK, :])
                    dk_rm[win, :] += lax.dot_general(ds, stacked(q_rm, rows), tn_dims, preferred_element_type=F32)
                    dv_rm[win, :] += lax.dot_general(p, stacked(do_rm, rows), tn_dims, preferred_element_type=F32)

                scores(0, 0)

                def pair(i, carry):
                    b = 2 * i
                    grads(b, 0)
                    scores(b + 1, 1)
                    grads(b + 1, 1)
                    scores(b + 2, 0)
                    return carry

                lax.fori_loop(0, n_blk // 2 - 1, pair, 0)
                grads(n_blk - 2, 0)
                scores(n_blk - 1, 1)
                grads(n_blk - 1, 1)
                _from_residue_major(dq_ref, dq_rm, s, d)
                _from_residue_major(dk_ref, dk_rm, s, d, src_off=HALF_SPAN)
                _from_residue_major(dv_ref, dv_rm, s, d, src_off=HALF_SPAN)

    qk_spec = pl.BlockSpec((s, LANES), lambda hp, g: (0, g * n_pairs + hp))
    v_spec = pl.BlockSpec((s, LANES), lambda hp, g: (0, v_col0 + g * n_pairs + hp))
    o_spec = pl.BlockSpec((s, LANES), lambda hp, g: (0, hp))
    width = qn.shape[1]
    f32buf = pltpu.VMEM((s, LANES), F32)
    f32pad = pltpu.VMEM((spad, LANES), F32)
    return _pallas(
        body, name=name, grid=(n_pairs, len(DILATIONS)),
        in_specs=[qk_spec, qk_spec, v_spec, o_spec, o_spec, o_spec,
                  pl.BlockSpec((LANES, LANES), lambda hp, g: (0, 0))],
        out_specs=[qk_spec, qk_spec, qk_spec],
        out_shape=[jax.ShapeDtypeStruct((s, width), F32)] * 3,
        operands=[qn, kn, proj, dattn, attn, lse, bd],
        scratch_shapes=[pltpu.VMEM((s, LANES), BF16), pltpu.VMEM((spad, LANES), BF16),
                        pltpu.VMEM((spad, LANES), BF16), pltpu.VMEM((s, LANES), BF16),
                        f32buf, f32buf, f32buf, f32buf, f32buf, f32pad, f32pad,
                        f32buf, f32buf, f32buf, f32buf, pltpu.VMEM((QBLK, KWIN), F32),
                        pltpu.VMEM((2, 2 * QBLK, KWIN), BF16), pltpu.VMEM((2, 2 * QBLK, KWIN), BF16)],
        rider=rider)


CONV_PAD = 16


def _conv_fwd(proj, conv_w, conv_b, col0, name, rider=None):
    s = proj.shape[0]
    ch = conv_w.shape[1]
    nblk = ch // LANES
    a0 = col0 // LANES
    tr = 256
    shift = CONV_PAD - (CONV_WIDTH - 1) // 2

    def body(a_ref, b_ref, w_ref, bias_ref, u0_ref, uc_ref, pad):
        z = jnp.zeros((CONV_PAD, LANES), F32)
        pad[0:CONV_PAD, :] = z
        pad[s + CONV_PAD: s + 2 * CONV_PAD, :] = z

        def glu(rows):
            u0 = a_ref[rows, :] * jax.nn.sigmoid(b_ref[rows, :])
            u0_ref[rows, :] = u0
            pad[pl.ds(rows.start + CONV_PAD, rows.size), :] = u0

        _row_chunks(s, glu)
        for t in range(0, s, tr):
            acc = jnp.broadcast_to(bias_ref[...], (tr, LANES))
            for k in range(CONV_WIDTH):
                acc = acc + w_ref[k:k + 1, :] * pad[t + k + shift: t + k + shift + tr, :]
            uc_ref[t:t + tr, :] = acc

    return _pallas(
        body, name=name, grid=(nblk,),
        in_specs=[pl.BlockSpec((s, LANES), lambda c: (0, a0 + c)),
                  pl.BlockSpec((s, LANES), lambda c: (0, a0 + nblk + c)),
                  pl.BlockSpec((CONV_WIDTH, LANES), lambda c: (0, c)),
                  pl.BlockSpec((1, LANES), lambda c: (0, c))],
        out_specs=[pl.BlockSpec((s, LANES), lambda c: (0, c))] * 2,
        out_shape=[jax.ShapeDtypeStruct((s, ch), F32)] * 2, operands=[proj, proj, conv_w, conv_b],
        scratch_shapes=[pltpu.VMEM((s + 2 * CONV_PAD, LANES), F32)], rider=rider)


def _ln_silu_fwd(uc, ln_w, ln_b, name):
    s, ch = uc.shape
    tm = 256

    def body(u_ref, w_ref, b_ref, o_ref):
        u = u_ref[...]
        mu = jnp.mean(u, axis=-1, keepdims=True)
        xc = u - mu
        rstd = lax.rsqrt(jnp.mean(xc * xc, axis=-1, keepdims=True) + EPS)
        z = xc * rstd * w_ref[...] + b_ref[...]
        o_ref[...] = (z * jax.nn.sigmoid(z)).astype(BF16)

    row = pl.BlockSpec((tm, ch), lambda i: (i, 0))
    vec = pl.BlockSpec((1, ch), lambda i: (0, 0))
    return pl.pallas_call(
        body, name=name, grid=(s // tm,), in_specs=[row, vec, vec], out_specs=row,
        out_shape=jax.ShapeDtypeStruct((s, ch), BF16), compiler_params=_params(),
    )(uc, ln_w, ln_b)


def _ln_silu_bwd(du3, uc, ln_w, ln_b, name):
    s, ch = uc.shape
    tm = 256

    def body(d_ref, u_ref, w_ref, b_ref, du_ref, dw_ref, db_ref):
        u = u_ref[...]
        mu = jnp.mean(u, axis=-1, keepdims=True)
        xc = u - mu
        rstd = lax.rsqrt(jnp.mean(xc * xc, axis=-1, keepdims=True) + EPS)
        xhat = xc * rstd
        z = xhat * w_ref[...] + b_ref[...]
        sg = jax.nn.sigmoid(z)
        dz = d_ref[...] * (sg * (1.0 + z * (1.0 - sg)))
        dxh = dz * w_ref[...]
        du_ref[...] = rstd * (dxh - jnp.mean(dxh, axis=-1, keepdims=True)
                              - xhat * jnp.mean(dxh * xhat, axis=-1, keepdims=True))
        pw = jnp.sum(dz * xhat, axis=0, keepdims=True)
        pb = jnp.sum(dz, axis=0, keepdims=True)
        first = pl.program_id(0) == 0

        @pl.when(first)
        def _():
            dw_ref[...] = pw
            db_ref[...] = pb

        @pl.when(jnp.logical_not(first))
        def _():
            dw_ref[...] += pw
            db_ref[...] += pb

    row = pl.BlockSpec((tm, ch), lambda i: (i, 0))
    vec = pl.BlockSpec((1, ch), lambda i: (0, 0))
    return pl.pallas_call(
        body, name=name, grid=(s // tm,), in_specs=[row, row, vec, vec], out_specs=[row, vec, vec],
        out_shape=[jax.ShapeDtypeStruct((s, ch), F32), jax.ShapeDtypeStruct((1, ch), F32),
                   jax.ShapeDtypeStruct((1, ch), F32)],
        compiler_params=_params(),
    )(du3, uc, ln_w, ln_b)


def _conv_bwd(duc, u0, proj, conv_w, col0, name, rider=None):
    s = proj.shape[0]
    ch = conv_w.shape[1]
    nblk = ch // LANES
    a0 = col0 // LANES
    tr = 256
    half = (CONV_WIDTH - 1) // 2
    shift = CONV_PAD - half

    def body(duc_ref, u0_ref, a_ref, b_ref, w_ref, da_ref, db_ref, dw_ref, dbias_ref, pad_d, pad_u):
        z = jnp.zeros((CONV_PAD, LANES), F32)
        for buf in (pad_d, pad_u):
            buf[0:CONV_PAD, :] = z
            buf[s + CONV_PAD: s + 2 * CONV_PAD, :] = z

        def fill(rows):
            dst = pl.ds(rows.start + CONV_PAD, rows.size)
            pad_d[dst, :] = duc_ref[rows, :]
            pad_u[dst, :] = u0_ref[rows, :]

        _row_chunks(s, fill)
        dw_acc = [jnp.zeros((8, LANES), F32) for _ in range(CONV_WIDTH)]
        dbias_acc = jnp.zeros((8, LANES), F32)
        for t in range(0, s, tr):
            d_t = duc_ref[t:t + tr, :]
            dbias_acc = dbias_acc + jnp.sum(d_t.reshape(tr // 8, 8, LANES), axis=0)
            du0 = jnp.zeros((tr, LANES), F32)
            for k in range(CONV_WIDTH):
                du0 = du0 + w_ref[k:k + 1, :] * pad_d[t - k + half + CONV_PAD: t - k + half + CONV_PAD + tr, :]
                prod = d_t * pad_u[t + k + shift: t + k + shift + tr, :]
                dw_acc[k] = dw_acc[k] + jnp.sum(prod.reshape(tr // 8, 8, LANES), axis=0)
            av = a_ref[t:t + tr, :]
            sg = jax.nn.sigmoid(b_ref[t:t + tr, :])
            da_ref[t:t + tr, :] = (du0 * sg).astype(BF16)
            db_ref[t:t + tr, :] = (du0 * av * sg * (1.0 - sg)).astype(BF16)
        for k in range(CONV_WIDTH):
            dw_ref[k:k + 1, :] = jnp.sum(dw_acc[k], axis=0, keepdims=True)
        dbias_ref[...] = jnp.sum(dbias_acc, axis=0, keepdims=True)

    col = lambda off: pl.BlockSpec((s, LANES), lambda c: (0, off + c))
    return _pallas(
        body, name=name, grid=(nblk,),
        in_specs=[col(0), col(0), col(a0), col(a0 + nblk),
                  pl.BlockSpec((CONV_WIDTH, LANES), lambda c: (0, c))],
        out_specs=[col(0), col(0), pl.BlockSpec((CONV_WIDTH, LANES), lambda c: (0, c)),
                   pl.BlockSpec((1, LANES), lambda c: (0, c))],
        out_shape=[jax.ShapeDtypeStruct((s, ch), BF16)] * 2
        + [jax.ShapeDtypeStruct((CONV_WIDTH, ch), F32), jax.ShapeDtypeStruct((1, ch), F32)],
        operands=[duc, u0, proj, proj, conv_w],
        scratch_shapes=[pltpu.VMEM((s + 2 * CONV_PAD, LANES), F32)] * 2, rider=rider)


GATE_BLK = 512


def _gate_fwd(proj, bg, y_a, y_b, col0, name):
    s, d = y_a.shape
    tm = 256
    g0 = col0 // GATE_BLK
    nb = d // GATE_BLK

    def body(ga_ref, gb_ref, ba_ref, bb_ref, ya_ref, yb_ref, o_ref):
        g_a = jax.nn.sigmoid(ga_ref[...] + ba_ref[...])
        g_b = jax.nn.sigmoid(gb_ref[...] + bb_ref[...])
        o_ref[...] = (g_a * ya_ref[...] + g_b * yb_ref[...]).astype(BF16)

    act = pl.BlockSpec((tm, GATE_BLK), lambda i, j: (i, j))
    return pl.pallas_call(
        body, name=name, grid=(s // tm, nb),
        in_specs=[pl.BlockSpec((tm, GATE_BLK), lambda i, j: (i, g0 + j)),
                  pl.BlockSpec((tm, GATE_BLK), lambda i, j: (i, g0 + nb + j)),
                  pl.BlockSpec((None, 1, GATE_BLK), lambda i, j: (0, 0, j)),
                  pl.BlockSpec((None, 1, GATE_BLK), lambda i, j: (1, 0, j)), act, act],
        out_specs=act, out_shape=jax.ShapeDtypeStruct((s, d), BF16), compiler_params=_params(),
    )(proj, proj, bg, bg, y_a, y_b)


def _gate_bwd(d_mixed, proj, bg, y_a, y_b, col0, name, rider=None):
    s, d = y_a.shape
    tm = 256
    half = d // 2
    assert col0 % half == 0
    c0 = col0 // half

    def body(dm_ref, a0_ref, a1_ref, b0_ref, b1_ref, bias_ref, ya_ref, yb_ref, dgl_ref, dya_ref, dyb_ref, db_ref):
        dm = dm_ref[...]
        parts = []
        for br, (lo_ref, hi_ref, y_ref, dy_ref) in enumerate(((a0_ref, a1_ref, ya_ref, dya_ref),
                                                              (b0_ref, b1_ref, yb_ref, dyb_ref))):
            logits = jnp.concatenate([lo_ref[...], hi_ref[...]], axis=1)
            gate = jax.nn.sigmoid(logits + bias_ref[br])
            dy_ref[...] = (dm * gate).astype(BF16)
            dgl = dm * y_ref[...] * gate * (1.0 - gate)
            dgl_ref[:, br * d:(br + 1) * d] = dgl.astype(BF16)
            parts.append(jnp.sum(dgl, axis=0, keepdims=True))
        part = jnp.concatenate(parts, axis=0)
        first = pl.program_id(0) == 0

        @pl.when(first)
        def _():
            db_ref[...] = part

        @pl.when(jnp.logical_not(first))
        def _():
            db_ref[...] += part

    row = pl.BlockSpec((tm, d), lambda i: (i, 0))
    logit_blk = lambda k: pl.BlockSpec((tm, half), functools.partial(lambda i, k: (i, c0 + k), k=k))
    return _pallas(
        body, name=name, grid=(s // tm,),
        in_specs=[row, logit_blk(0), logit_blk(1), logit_blk(2), logit_blk(3),
                  pl.BlockSpec((2, 1, d), lambda i: (0, 0, 0)), row, row],
        out_specs=[pl.BlockSpec((tm, 2 * d), lambda i: (i, 0)), row, row, pl.BlockSpec((2, d), lambda i: (0, 0))],
        out_shape=[jax.ShapeDtypeStruct((s, 2 * d), BF16), jax.ShapeDtypeStruct((s, d), BF16),
                   jax.ShapeDtypeStruct((s, d), BF16), jax.ShapeDtypeStruct((2, d), F32)],
        operands=[d_mixed, proj, proj, proj, proj, bg, y_a, y_b], rider=rider)


def _ffn_in_swiglu(h2, w_blocked, name):
    s, k = h2.shape
    nblk, _, tn = w_blocked.shape
    ff = nblk // 2 * tn
    tm = 512

    def body(a_ref, wg_ref, wu_ref, g_ref, u_ref, act_ref):
        a = a_ref[...]
        gt = jnp.dot(a, wg_ref[...], preferred_element_type=F32)
        up = jnp.dot(a, wu_ref[...], preferred_element_type=F32)
        g_ref[...] = gt
        u_ref[...] = up
        act_ref[...] = (gt * jax.nn.sigmoid(gt) * up).astype(BF16)

    out = pl.BlockSpec((tm, tn), lambda i, j: (i, j))
    return pl.pallas_call(
        body, name=name, grid=(s // tm, nblk // 2),
        in_specs=[pl.BlockSpec((tm, k), lambda i, j: (i, 0)),
                  pl.BlockSpec((None, k, tn), lambda i, j: (j, 0, 0)),
                  pl.BlockSpec((None, k, tn), lambda i, j: (nblk // 2 + j, 0, 0))],
        out_specs=[out, out, out],
        out_shape=[jax.ShapeDtypeStruct((s, ff), F32), jax.ShapeDtypeStruct((s, ff), F32),
                   jax.ShapeDtypeStruct((s, ff), BF16)],
        compiler_params=_params(),
    )(h2, w_blocked, w_blocked)


def _swiglu_bwd(gate, up, d_act, name, rider=None):
    s, ff = gate.shape
    tm = 256

    def body(g_ref, u_ref, d_ref, o_ref):
        gt = g_ref[...]
        sg = jax.nn.sigmoid(gt)
        dv = d_ref[...]
        o_ref[:, 0:ff] = (dv * u_ref[...] * (sg * (1.0 + gt * (1.0 - sg)))).astype(BF16)
        o_ref[:, ff:2 * ff] = (dv * gt * sg).astype(BF16)

    row = pl.BlockSpec((tm, ff), lambda i: (i, 0))
    return _pallas(
        body, name=name, grid=(s // tm,), in_specs=[row, row, row],
        out_specs=pl.BlockSpec((tm, 2 * ff), lambda i: (i, 0)),
        out_shape=jax.ShapeDtypeStruct((s, 2 * ff), BF16), operands=[gate, up, d_act], rider=rider)


def _loss_fwd_bwd(y, target, name):
    s, d = y.shape
    tm = 256

    def body(y_ref, t_ref, dy_ref, dyb_ref, loss_ref, acc):
        diff = y_ref[...] - t_ref[...]
        dy = diff * (1.0 / d)
        dy_ref[...] = dy
        dyb_ref[...] = dy.astype(BF16)
        part = jnp.sum((diff * diff).reshape(tm // 8, 8, d), axis=0)
        i = pl.program_id(0)

        @pl.when(i == 0)
        def _():
            acc[...] = part

        @pl.when(i > 0)
        def _():
            acc[...] += part

        @pl.when(i == pl.num_programs(0) - 1)
        def _():
            loss_ref[...] = (0.5 / d) * jnp.sum(jnp.sum(acc[...], axis=1, keepdims=True), axis=0, keepdims=True)

    row = pl.BlockSpec((tm, d), lambda i: (i, 0))
    return pl.pallas_call(
        body, name=name, grid=(s // tm,), in_specs=[row, row],
        out_specs=[row, row, pl.BlockSpec((1, 1), lambda i: (0, 0))],
        out_shape=[jax.ShapeDtypeStruct((s, d), F32), jax.ShapeDtypeStruct((s, d), BF16),
                   jax.ShapeDtypeStruct((1, 1), F32)],
        scratch_shapes=[pltpu.VMEM((8, d), F32)], compiler_params=_params(),
    )(y, target)


LATE_GATHER = ("w_o_attn", "w_pw_conv", "w_out", "w_ffn_in", "w_ffn_out")
EARLY_REDUCE = LATE_GATHER


def _blocks_by_half(g):
    if g.ndim == 2:
        g = g.reshape(N_CHIPS, g.shape[0] // N_CHIPS, g.shape[1])
    return g.reshape(N_CHIPS, 2, g.shape[1] // 2, g.shape[2])


def _forward_backward(x, pos_col, target, wts, late_bufs, pos_arr):
    wts = dict(wts)
    consts = _rope_consts()
    bd = consts[3]
    qw2 = jnp.tile(wts["q_norm_w"], (1, LANES // HEAD_DIM))
    kw2 = jnp.tile(wts["k_norm_w"], (1, LANES // HEAD_DIM))
    qkv_w = 3 * N_SLOT_HEADS * HEAD_DIM
    conv_col0 = 3 * qkv_w
    ch = wts["conv_w"].shape[1]
    gate_col0 = conv_col0 + 2 * ch

    h = _rmsnorm_fwd(x, wts["norm1_w"], "rms1_fwd")
    late = dict(zip(LATE_GATHER, late_bufs))
    quarter = late["w_ffn_in"].shape[2] // 4
    proj, (late["w_o_attn"], late["w_pw_conv"], late["w_out"], late["w_ffn_in"]) = _matmul(
        h, wts["w_in"], mode="nn", tm=512, tn=1920, tk=1024, out_dtype=F32, name="mm_proj", b_blocked=True,
        rider=_gather_ici_rider([late["w_o_attn"], late["w_pw_conv"], late["w_out"], late["w_ffn_in"]], [],
                                row_ranges=[None, None, None, (0, quarter)]))
    (qn, kn), (late["w_ffn_in"],) = _qk_fwd(
        proj, pos_col, qw2, kw2, consts, "qk_fwd",
        rider=_gather_ici_rider([late["w_ffn_in"]], [], row_ranges=[(quarter, 2 * quarter)]))
    (attn, lse), (late["w_ffn_in"], late["w_ffn_out"]) = _attn_fwd(
        qn, kn, proj, "attn_fwd",
        rider=_gather_ici_rider([late["w_ffn_in"], late["w_ffn_out"]], [],
                                row_ranges=[(3 * quarter, quarter), None]))
    (u0, uc), late_bufs = _conv_fwd(proj, wts["conv_w"], wts["conv_b"], conv_col0, "conv_fwd",
                                    rider=_gather_forward_rider([late[n] for n in LATE_GATHER]))
    for n, buf in zip(LATE_GATHER, late_bufs):
        full = buf.reshape(N_CHIPS, -1, buf.shape[3])
        wts[n] = full.reshape(-1, full.shape[2]) if n in ROW_SHARDED else full
    attn_b = attn.astype(BF16)
    y_a = _matmul(attn_b, wts["w_o_attn"], mode="nn", tm=1024, tn=256, tk=512, out_dtype=F32, name="mm_ya",
                  b_blocked=True)
    u3 = _ln_silu_fwd(uc, wts["conv_ln_w"], wts["conv_ln_b"], "ln_fwd")
    y_b = _matmul(u3, wts["w_pw_conv"], mode="nn", tm=1024, tn=256, tk=512, out_dtype=F32, name="mm_yb",
                  b_blocked=True)
    mixed = _gate_fwd(proj, wts["b_gate"], y_a, y_b, gate_col0, "gate_fwd")
    x1 = _matmul(mixed, wts["w_out"], mode="nn", tm=512, tn=1024, tk=1024, out_dtype=F32, name="mm_x1",
                 residual=x)
    h2 = _rmsnorm_fwd(x1, wts["norm2_w"], "rms2_fwd")
    gate, up, act = _ffn_in_swiglu(h2, wts["w_ffn_in"], "mm_gu_swiglu")
    x2 = _matmul(act, wts["w_ffn_out"], mode="nn", tm=512, tn=1024, tk=2816, out_dtype=F32, name="mm_x2",
                 residual=x1)
    dy, dy_b16, loss = _loss_fwd_bwd(x2, target, "loss")

    g = {}
    by_chip = {}

    def pair_add(n, blocks, received):
        return _add_own_half(blocks, received, pos_arr, f"grads_pair_add_{n}")

    d_act = _matmul(dy_b16, wts["w_ffn_out"], mode="nt", tm=512, tn=1408, tk=1024, out_dtype=F32, name="mm_dact")
    g_ffn_out = _blocks_by_half(
        _matmul(act, dy_b16, mode="tn", tm=1408, tn=1024, tk=2048, out_dtype=F32, name="mm_dwffnout"))
    dgu, (received,) = _swiglu_bwd(gate, up, d_act, "swiglu_bwd",
                                   rider=_pair_exchange_rider([g_ffn_out], halved=True))
    to_send, own = pair_add("w_ffn_out", g_ffn_out, received)
    dh2, (by_chip["w_ffn_out"],) = _matmul(
        dgu, wts["w_ffn_in"], mode="nt", tm=1024, tn=1024, tk=1408, out_dtype=F32, name="mm_dh2", b_blocked=True,
        rider=_chip_exchange_rider([to_send], [own]))
    g_ffn_in = _blocks_by_half(_matmul(h2, dgu, mode="tn", tm=512, tn=1408, tk=2048, out_dtype=F32,
                                       name="mm_dwffnin", out_blocked=N_CHIPS))
    dx1, dx1_b16, g["norm2_w"] = _rmsnorm_bwd(dh2, x1, wts["norm2_w"], dy, "rms2_bwd")
    d_mixed = _matmul(dx1_b16, wts["w_out"], mode="nt", tm=512, tn=1024, tk=1024, out_dtype=F32, name="mm_dmixed")
    g["w_out"] = _matmul(mixed, dx1_b16, mode="tn", tm=512, tn=1024, tk=2048, out_dtype=F32, name="mm_dwout")
    (dgl, dy_a, dy_b, g["b_gate"]), (received,) = _gate_bwd(
        d_mixed, proj, wts["b_gate"], y_a, y_b, gate_col0, "gate_bwd",
        rider=_pair_exchange_rider([g_ffn_in], halved=True))
    ffn_in_to_send, ffn_in_own = pair_add("w_ffn_in", g_ffn_in, received)
    dattn = _matmul(dy_a, wts["w_o_attn"], mode="nt", tm=1024, tn=512, tk=256, out_dtype=F32, name="mm_dattn",
                    b_blocked=True)
    g["w_o_attn"] = _matmul(attn_b, dy_a, mode="tn", tm=512, tn=256, tk=2048, out_dtype=F32, name="mm_dwo",
                            out_blocked=N_CHIPS)
    du3 = _matmul(dy_b, wts["w_pw_conv"], mode="nt", tm=1024, tn=512, tk=256, out_dtype=F32, name="mm_du3",
                  b_blocked=True)
    g["w_pw_conv"] = _matmul(u3, dy_b, mode="tn", tm=512, tn=256, tk=2048, out_dtype=F32, name="mm_dwpw",
                             out_blocked=N_CHIPS)
    duc, g["conv_ln_w"], g["conv_ln_b"] = _ln_silu_bwd(du3, uc, wts["conv_ln_w"], wts["conv_ln_b"], "ln_bwd")

    small3 = ("w_out", "w_o_attn", "w_pw_conv")
    g_small3 = [_blocks_by_half(g.pop(n)) for n in small3]
    (da, db, g["conv_w"], g["conv_b"]), received = _conv_bwd(
        duc, u0, proj, wts["conv_w"], conv_col0, "conv_bwd", rider=_pair_exchange_rider(g_small3, halved=True))
    sums3 = [pair_add(n, gb, rv) for n, gb, rv in zip(small3, g_small3, received)]
    (dqn, dkn, dv), (by_chip["w_ffn_in"],) = _attn_bwd(
        qn, kn, proj, dattn, attn, lse, bd, "attn_bwd",
        rider=_chip_exchange_rider([ffn_in_to_send], [ffn_in_own]))
    (dproj, dqw, dkw), exchanged3 = _qk_bwd(
        dqn, dkn, dv, da, db, dgl, proj, pos_col, qw2, kw2, consts, "qk_bwd",
        rider=_chip_exchange_rider([s[0] for s in sums3], [s[1] for s in sums3]))
    by_chip.update(zip(small3, exchanged3))
    halves = [_sum_chips(by_chip[n], pos_arr, f"grads_chip_sum_{n}") for n in EARLY_REDUCE]
    g["q_norm_w"] = dqw[:, :HEAD_DIM]
    g["k_norm_w"] = dkw[:, :HEAD_DIM]

    c = pos_arr[0]
    rh = h.shape[1] // 2
    h_sibling = lax.dynamic_slice_in_dim(h, (1 - c) * rh, rh, axis=1)
    h_own = lax.dynamic_slice_in_dim(h, c * rh, rh, axis=1)
    g_sibling, shards = _matmul(h_sibling, dproj, mode="tn", tm=rh, tn=1920, tk=2048, out_dtype=F32,
                                name="mm_dwin_sibling", out_blocked=N_CHIPS, rider=_pair_gather_rider(halves))
    reduced = dict(zip(EARLY_REDUCE, shards))
    g_own, from_sibling = _matmul(h_own, dproj, mode="tn", tm=rh, tn=1920, tk=2048, out_dtype=F32,
                                  name="mm_dwin_own", out_blocked=N_CHIPS,
                                  rider=_pair_exchange_rider([g_sibling], halved=False))
    to_send, own = _add_own_half(g_own, from_sibling[0], pos_arr, "grads_pair_add_w_in")
    first_rows = 288
    dh, by_chip_w_in = _matmul(dproj, wts["w_in"], mode="nt", tm=1024, tn=1024, tk=1920, out_dtype=F32, name="mm_dh",
                               b_blocked=True,
                               rider=_chip_exchange_rider([to_send], [own], row_range=(0, first_rows)))
    (grad_x, _, g["norm1_w"]), by_chip_w_in = _rmsnorm_bwd(
        dh, x, wts["norm1_w"], dx1, "rms1_bwd",
        rider=_chip_exchange_rider([to_send], by_chip_w_in, row_range=(first_rows, rh - first_rows)))
    return loss, grad_x, g, reduced, by_chip_w_in[0]


def _mesh_pos():
    return lax.axis_index("x"), lax.axis_index("y"), lax.axis_index("c")


def _other_chips(x, y):
    return [(1 - x, y), (x, 1 - y), (1 - x, 1 - y)]


def _cast_into_slot(shard, chip_arr, dtype, name):
    r, c = shard.shape
    tr = r // 2 if r % 32 == 0 else r

    def body(chip_ref, s_ref, o_ref):
        del chip_ref
        o_ref[...] = s_ref[...].astype(dtype)

    return pl.pallas_call(
        body, name=name,
        grid_spec=pltpu.PrefetchScalarGridSpec(
            num_scalar_prefetch=1, grid=(r // tr,),
            in_specs=[pl.BlockSpec((tr, c), lambda i, chip_ref: (i, 0))],
            out_specs=pl.BlockSpec((None, tr, c), lambda i, chip_ref: (chip_ref[0], i, 0))),
        out_shape=jax.ShapeDtypeStruct((N_CHIPS, r, c), dtype), compiler_params=_params(),
    )(chip_arr, shard)


def _allgather_inplace(big, small, name):
    nb, ns = len(big), len(small)
    n = nb + ns

    def body(*refs):
        bufs = refs[n:2 * n]
        send_sems, recv_sems, fsend_sems, frecv_sems = refs[2 * n:]
        x, y, c = _mesh_pos()
        me = 2 * x + y
        chips = _other_chips(x, y)

        def part(a, slot, half):
            return bufs[a].at[slot, half] if a < nb else bufs[a].at[slot]

        sends = []
        for a in range(n):
            for k, (px, py) in enumerate(chips):
                cp = pltpu.make_async_remote_copy(
                    src_ref=part(a, me, c), dst_ref=part(a, me, c), send_sem=send_sems.at[a, k],
                    recv_sem=recv_sems.at[a, k], device_id=(px, py, c), device_id_type=MESH)
                cp.start()
                sends.append(cp)
        for a in range(n):
            for k, (px, py) in enumerate(chips):
                slot = 2 * px + py
                pltpu.make_async_remote_copy(
                    src_ref=part(a, slot, c), dst_ref=part(a, slot, c), send_sem=send_sems.at[a, k],
                    recv_sem=recv_sems.at[a, k], device_id=(px, py, c), device_id_type=MESH).wait_recv()
                if a < nb:
                    fwd = pltpu.make_async_remote_copy(
                        src_ref=part(a, slot, c), dst_ref=part(a, slot, c), send_sem=fsend_sems.at[a, k],
                        recv_sem=frecv_sems.at[a, k], device_id=(x, y, 1 - c), device_id_type=MESH)
                    fwd.start()
                    sends.append(fwd)
        for a in range(nb):
            for k, (px, py) in enumerate(chips):
                slot = 2 * px + py
                pltpu.make_async_remote_copy(
                    src_ref=part(a, slot, 1 - c), dst_ref=part(a, slot, 1 - c), send_sem=fsend_sems.at[a, k],
                    recv_sem=frecv_sems.at[a, k], device_id=(x, y, 1 - c), device_id_type=MESH).wait_recv()
        for cp in sends:
            cp.wait_send()

    ops = list(big) + list(small)
    return pl.pallas_call(
        body, name=name, in_specs=[ANY] * n, out_specs=[ANY] * n,
        out_shape=[jax.ShapeDtypeStruct(o.shape, o.dtype) for o in ops],
        input_output_aliases={i: i for i in range(n)},
        scratch_shapes=[pltpu.SemaphoreType.DMA((n, 3)), pltpu.SemaphoreType.DMA((n, 3)),
                        pltpu.SemaphoreType.DMA((nb, 3)), pltpu.SemaphoreType.DMA((nb, 3))],
    )(*ops)


def _comm_call(rider, name):
    def body():
        pass

    return _pallas(body, name=name, grid=(1,), in_specs=[], out_specs=[], out_shape=[], operands=[],
                   rider=rider)[1]


def _gather_ici_rider(big, small, row_ranges=None):
    nb = len(big)
    n = nb + len(small)
    row_ranges = row_ranges or [None] * nb

    def copies(bufs, sems):
        x, y, c = _mesh_pos()
        me = 2 * x + y

        def part(a, slot):
            if a >= nb:
                return bufs[a].at[slot]
            if row_ranges[a] is None:
                return bufs[a].at[slot, c]
            return bufs[a].at[slot, c, pl.ds(*row_ranges[a])]
        out = []
        for a in range(n):
            for k, (px, py) in enumerate(_other_chips(x, y)):
                send = functools.partial(
                    pltpu.make_async_remote_copy,
                    src_ref=part(a, me), dst_ref=part(a, me), send_sem=sems[0].at[a, k],
                    recv_sem=sems[1].at[a, k], device_id=(px, py, c), device_id_type=MESH)
                recv = functools.partial(
                    pltpu.make_async_remote_copy,
                    src_ref=part(a, 2 * px + py), dst_ref=part(a, 2 * px + py), send_sem=sems[0].at[a, k],
                    recv_sem=sems[1].at[a, k], device_id=(px, py, c), device_id_type=MESH)
                out.append((send, recv))
        return out

    def start(r_in, r_out, sems):
        for send, _ in copies(r_out, sems):
            send().start()

    def wait(r_in, r_out, sems):
        cps = copies(r_out, sems)
        for _, recv in cps:
            recv().wait_recv()
        for send, _ in cps:
            send().wait_send()

    ops = list(big) + list(small)
    return _Rider(ops, [jax.ShapeDtypeStruct(o.shape, o.dtype) for o in ops], {i: i for i in range(n)},
                  [pltpu.SemaphoreType.DMA((n, 3)), pltpu.SemaphoreType.DMA((n, 3))], start, wait)


def _gather_forward_rider(big):
    n = len(big)

    def copies(bufs, sems):
        x, y, c = _mesh_pos()
        out = []
        for a in range(n):
            for k, (px, py) in enumerate(_other_chips(x, y)):
                slot = 2 * px + py
                send = functools.partial(
                    pltpu.make_async_remote_copy,
                    src_ref=bufs[a].at[slot, c], dst_ref=bufs[a].at[slot, c], send_sem=sems[0].at[a, k],
                    recv_sem=sems[1].at[a, k], device_id=(x, y, 1 - c), device_id_type=MESH)
                recv = functools.partial(
                    pltpu.make_async_remote_copy,
                    src_ref=bufs[a].at[slot, 1 - c], dst_ref=bufs[a].at[slot, 1 - c], send_sem=sems[0].at[a, k],
                    recv_sem=sems[1].at[a, k], device_id=(x, y, 1 - c), device_id_type=MESH)
                out.append((send, recv))
        return out

    def start(r_in, r_out, sems):
        for send, _ in copies(r_out, sems):
            send().start()

    def wait(r_in, r_out, sems):
        cps = copies(r_out, sems)
        for _, recv in cps:
            recv().wait_recv()
        for send, _ in cps:
            send().wait_send()

    return _Rider(big, [jax.ShapeDtypeStruct(o.shape, o.dtype) for o in big], {i: i for i in range(n)},
                  [pltpu.SemaphoreType.DMA((n, 3)), pltpu.SemaphoreType.DMA((n, 3))], start, wait)


def _pair_exchange_rider(gs, halved):
    n = len(gs)

    def copies(r_in, r_out, sems):
        x, y, c = _mesh_pos()
        return [pltpu.make_async_remote_copy(
            src_ref=r_in[a].at[:, 1 - c] if halved else r_in[a], dst_ref=r_out[a], send_sem=sems[0].at[a],
            recv_sem=sems[1].at[a], device_id=(x, y, 1 - c), device_id_type=MESH) for a in range(n)]

    def start(r_in, r_out, sems):
        for cp in copies(r_in, r_out, sems):
            cp.start()

    def wait(r_in, r_out, sems):
        for cp in copies(r_in, r_out, sems):
            cp.wait()

    return _Rider(gs, [jax.ShapeDtypeStruct((g.shape[0],) + g.shape[-2:], g.dtype) for g in gs], {},
                  [pltpu.SemaphoreType.DMA((n,)), pltpu.SemaphoreType.DMA((n,))], start, wait)


def _chip_exchange_rider(to_send, by_chip, row_range=None):
    n = len(to_send)

    def copies(r_in, r_out, sems):
        x, y, c = _mesh_pos()
        me = 2 * x + y
        rows = (lambda ref: ref) if row_range is None else (lambda ref: ref.at[pl.ds(*row_range)])
        out = []
        for a in range(n):
            for k, (px, py) in enumerate(_other_chips(x, y)):
                send = functools.partial(
                    pltpu.make_async_remote_copy,
                    src_ref=rows(r_in[a].at[2 * px + py]), dst_ref=rows(r_out[a].at[me]),
                    send_sem=sems[0].at[a, k], recv_sem=sems[1].at[a, k], device_id=(px, py, c),
                    device_id_type=MESH)
                recv = functools.partial(
                    pltpu.make_async_remote_copy,
                    src_ref=rows(r_in[a].at[me]), dst_ref=rows(r_out[a].at[2 * px + py]),
                    send_sem=sems[0].at[a, k], recv_sem=sems[1].at[a, k], device_id=(px, py, c),
                    device_id_type=MESH)
                out.append((send, recv))
        return out

    def start(r_in, r_out, sems):
        for send, _ in copies(r_in, r_out, sems):
            send().start()

    def wait(r_in, r_out, sems):
        cps = copies(r_in, r_out, sems)
        for _, recv in cps:
            recv().wait_recv()
        for send, _ in cps:
            send().wait_send()

    return _Rider(list(to_send) + list(by_chip), [jax.ShapeDtypeStruct(b.shape, b.dtype) for b in by_chip],
                  {n + i: i for i in range(n)},
                  [pltpu.SemaphoreType.DMA((n, 3)), pltpu.SemaphoreType.DMA((n, 3))], start, wait)


def _pair_gather_rider(bufs):
    n = len(bufs)

    def copies(r_out, sems):
        x, y, c = _mesh_pos()
        out = []
        for a in range(n):
            send = functools.partial(
                    pltpu.make_async_remote_copy,
                src_ref=r_out[a].at[c], dst_ref=r_out[a].at[c], send_sem=sems[0].at[a],
                recv_sem=sems[1].at[a], device_id=(x, y, 1 - c), device_id_type=MESH)
            recv = functools.partial(
                    pltpu.make_async_remote_copy,
                src_ref=r_out[a].at[1 - c], dst_ref=r_out[a].at[1 - c], send_sem=sems[0].at[a],
                recv_sem=sems[1].at[a], device_id=(x, y, 1 - c), device_id_type=MESH)
            out.append((send, recv))
        return out

    def start(r_in, r_out, sems):
        for send, _ in copies(r_out, sems):
            send().start()

    def wait(r_in, r_out, sems):
        cps = copies(r_out, sems)
        for _, recv in cps:
            recv().wait_recv()
        for send, _ in cps:
            send().wait_send()

    return _Rider(bufs, [jax.ShapeDtypeStruct(b.shape, b.dtype) for b in bufs], {i: i for i in range(n)},
                  [pltpu.SemaphoreType.DMA((n,)), pltpu.SemaphoreType.DMA((n,))], start, wait)


def _add_own_half(g, recv, pos_arr, name):
    nb, rh, cols = g.shape[0], g.shape[-2], g.shape[-1]

    def body(pos_ref, g_ref, r_ref, send_ref, own_ref):
        s = (g_ref[...] + r_ref[...]).astype(BF16)
        send_ref[...] = s

        @pl.when(pl.program_id(0) == pos_ref[1])
        def _():
            own_ref[...] = s

    blk = pl.BlockSpec((None, rh, cols), lambda j, pos_ref: (j, 0, 0))
    g_spec = blk if g.ndim == 3 else pl.BlockSpec((None, None, rh, cols),
                                                   lambda j, pos_ref: (j, pos_ref[0], 0, 0))
    shape = jax.ShapeDtypeStruct((nb, rh, cols), BF16)
    return pl.pallas_call(
        body, name=name,
        grid_spec=pltpu.PrefetchScalarGridSpec(
            num_scalar_prefetch=1, grid=(nb,), in_specs=[g_spec, blk],
            out_specs=[blk, pl.BlockSpec((None, rh, cols), lambda j, pos_ref: (pos_ref[1], 0, 0))]),
        out_shape=[shape, shape], compiler_params=_params(),
    )(pos_arr, g, recv)


def _sum_chips(gath, pos_arr, name):
    nb, rh, cols = gath.shape

    def body(pos_ref, a_ref, b_ref, c_ref, d_ref, o_ref):
        del pos_ref
        o_ref[...] = ((a_ref[...].astype(F32) + b_ref[...].astype(F32)) + c_ref[...].astype(F32)) \
            + d_ref[...].astype(F32)

    tr = rh // 2 if (rh // 2) % 16 == 0 else rh
    specs = [pl.BlockSpec((None, tr, cols), functools.partial(lambda i, pos_ref, j: (j, i, 0), j=j))
             for j in range(nb)]
    return pl.pallas_call(
        body, name=name,
        grid_spec=pltpu.PrefetchScalarGridSpec(
            num_scalar_prefetch=1, grid=(rh // tr,), in_specs=specs,
            out_specs=pl.BlockSpec((None, tr, cols), lambda i, pos_ref: (pos_ref[0], i, 0))),
        out_shape=jax.ShapeDtypeStruct((2, rh, cols), F32), compiler_params=_params(),
    )(pos_arr, gath, gath, gath, gath)


def _small_allreduce(v, name, rider=None):
    n = v.shape[0]
    n_dev = 8

    def body(v_ref, o_ref, buf, send_sems, recv_sems):
        x, y, c = _mesh_pos()
        me = 4 * x + 2 * y + c
        buf[me] = v_ref[...]
        sends = []
        peers = []
        for r in range(1, n_dev):
            px = 1 - x if r & 4 else x
            py = 1 - y if r & 2 else y
            pc = 1 - c if r & 1 else c
            peers.append((px, py, pc))
            cp = pltpu.make_async_remote_copy(
                src_ref=v_ref, dst_ref=buf.at[me], send_sem=send_sems.at[r - 1],
                recv_sem=recv_sems.at[r - 1], device_id=(px, py, pc), device_id_type=MESH)
            cp.start()
            sends.append(cp)
        for r, (px, py, pc) in enumerate(peers):
            pltpu.make_async_remote_copy(
                src_ref=v_ref, dst_ref=buf.at[4 * px + 2 * py + pc], send_sem=send_sems.at[r],
                recv_sem=recv_sems.at[r], device_id=(px, py, pc), device_id_type=MESH).wait_recv()
        for cp in sends:
            cp.wait_send()
        acc = buf[0]
        for i in range(1, n_dev):
            acc = acc + buf[i]
        o_ref[...] = acc

    whole = pl.BlockSpec(v.shape, lambda i: (0, 0))
    return _pallas(
        body, name=name, grid=(1,), in_specs=[whole], out_specs=whole,
        out_shape=jax.ShapeDtypeStruct(v.shape, v.dtype), operands=[v],
        scratch_shapes=[pltpu.VMEM((n_dev, n, LANES), F32), pltpu.SemaphoreType.DMA((n_dev - 1,)),
                        pltpu.SemaphoreType.DMA((n_dev - 1,))],
        rider=rider)


def _adamw_math(w, g, m, v):
    m = ADAM_B1 * m + (1.0 - ADAM_B1) * g
    v = ADAM_B2 * v + (1.0 - ADAM_B2) * (g * g)
    m_hat = m / (1.0 - ADAM_B1 ** ADAM_STEP)
    v_hat = v / (1.0 - ADAM_B2 ** ADAM_STEP)
    delta = -ADAM_LR * (m_hat / (jnp.sqrt(v_hat) + ADAM_EPS) + ADAM_WD * w)
    return delta, m, v


def _adamw(w, g, m, v, name):
    r, c = w.shape
    tr = 128 if r % 128 == 0 else 64
    assert r % tr == 0

    def body(w_ref, g_ref, m_ref, v_ref, go_ref, d_ref, mo_ref, vo_ref):
        gv = g_ref[...]
        d, mn, vn = _adamw_math(w_ref[...], gv, m_ref[...], v_ref[...])
        go_ref[...] = gv
        d_ref[...] = d
        mo_ref[...] = mn
        vo_ref[...] = vn

    blk = pl.BlockSpec((tr, c), lambda i: (i, 0))
    return _pallas(body, name=name, grid=(r // tr,), in_specs=[blk] * 4, out_specs=[blk] * 4,
                   out_shape=[jax.ShapeDtypeStruct((r, c), F32)] * 4, operands=[w, g, m, v])


def _adamw_small(ws, gs, ms, vs, name):
    n = len(ws)

    def body(*refs):
        w_r, g_r, m_r, v_r = refs[:n], refs[n:2 * n], refs[2 * n:3 * n], refs[3 * n:4 * n]
        d_o, m_o, v_o = refs[4 * n:5 * n], refs[5 * n:6 * n], refs[6 * n:7 * n]
        for i in range(n):
            d, mn, vn = _adamw_math(w_r[i][...], g_r[i][...], m_r[i][...], v_r[i][...])
            d_o[i][...] = d
            m_o[i][...] = mn
            v_o[i][...] = vn

    vm = pl.BlockSpec(memory_space=pltpu.VMEM)
    shapes = [jax.ShapeDtypeStruct(w.shape, F32) for w in ws]
    outs = pl.pallas_call(
        body, name=name, in_specs=[vm] * (4 * n), out_specs=[vm] * (3 * n), out_shape=shapes * 3,
        compiler_params=_params(),
    )(*ws, *gs, *ms, *vs)
    return outs[:n], outs[n:2 * n], outs[2 * n:]


BIG = ("w_in", "w_o_attn", "w_pw_conv", "w_out", "w_ffn_in", "w_ffn_out")
ROW_SHARDED = ("w_out", "w_ffn_out")
SMALL = ("norm1_w", "b_gate", "q_norm_w", "k_norm_w", "conv_w", "conv_b", "conv_ln_w", "conv_ln_b", "norm2_w")
ORDER = ("norm1_w", "w_in", "b_gate", "q_norm_w", "k_norm_w", "w_o_attn", "conv_w", "conv_b", "conv_ln_w",
         "conv_ln_b", "w_pw_conv", "w_out", "norm2_w", "w_ffn_in", "w_ffn_out")
PACK_TILE = 8 * LANES


def _pack_small(parts):
    rows = []
    for p in parts:
        flat = p.reshape(-1)
        pad = (-flat.shape[0]) % PACK_TILE
        rows.append(jnp.pad(flat, (0, pad)).reshape(-1, LANES))
    return jnp.concatenate(rows, axis=0)


def _unpack_small(packed, shapes):
    out, row = [], 0
    for shp in shapes:
        size = int(np.prod(shp))
        nrow = -(-size // PACK_TILE) * (PACK_TILE // LANES)
        out.append(packed[row:row + nrow].reshape(-1)[:size].reshape(shp))
        row += nrow
    return out


def kernel(x, positions, norm1_w, w_in, b_gate, q_norm_w, k_norm_w, w_o_attn, conv_w, conv_b, conv_ln_w, conv_ln_b, w_pw_conv, w_out, norm2_w, w_ffn_in, w_ffn_out, loss_target, m_norm1_w, m_w_in, m_b_gate, m_q_norm_w, m_k_norm_w, m_w_o_attn, m_conv_w, m_conv_b, m_conv_ln_w, m_conv_ln_b, m_w_pw_conv, m_w_out, m_norm2_w, m_w_ffn_in, m_w_ffn_out, v_norm1_w, v_w_in, v_b_gate, v_q_norm_w, v_k_norm_w, v_w_o_attn, v_conv_w, v_conv_b, v_conv_ln_w, v_conv_ln_b, v_w_pw_conv, v_w_out, v_norm2_w, v_w_ffn_in, v_w_ffn_out):
    w = dict(norm1_w=norm1_w, w_in=w_in, b_gate=b_gate, q_norm_w=q_norm_w, k_norm_w=k_norm_w, w_o_attn=w_o_attn,
             conv_w=conv_w, conv_b=conv_b, conv_ln_w=conv_ln_w, conv_ln_b=conv_ln_b, w_pw_conv=w_pw_conv,
             w_out=w_out, norm2_w=norm2_w, w_ffn_in=w_ffn_in, w_ffn_out=w_ffn_out)
    m = dict(norm1_w=m_norm1_w, w_in=m_w_in, b_gate=m_b_gate, q_norm_w=m_q_norm_w, k_norm_w=m_k_norm_w,
             w_o_attn=m_w_o_attn, conv_w=m_conv_w, conv_b=m_conv_b, conv_ln_w=m_conv_ln_w,
             conv_ln_b=m_conv_ln_b, w_pw_conv=m_w_pw_conv, w_out=m_w_out, norm2_w=m_norm2_w,
             w_ffn_in=m_w_ffn_in, w_ffn_out=m_w_ffn_out)
    v = dict(norm1_w=v_norm1_w, w_in=v_w_in, b_gate=v_b_gate, q_norm_w=v_q_norm_w, k_norm_w=v_k_norm_w,
             w_o_attn=v_w_o_attn, conv_w=v_conv_w, conv_b=v_conv_b, conv_ln_w=v_conv_ln_w,
             conv_ln_b=v_conv_ln_b, w_pw_conv=v_w_pw_conv, w_out=v_w_out, norm2_w=v_norm2_w,
             w_ffn_in=v_w_ffn_in, w_ffn_out=v_w_ffn_out)
    cx, cy, cc = _mesh_pos()
    chip = 2 * cx + cy

    chip_arr = chip.reshape(1).astype(jnp.int32)
    pos_arr = jnp.stack([cc, chip]).astype(jnp.int32)
    bufs = {}
    for n in BIG:
        buf = _cast_into_slot(w[n][0], chip_arr, BF16, f"cast_{n}")
        bufs[n] = buf.reshape(N_CHIPS, 2, buf.shape[1] // 2, buf.shape[2])
    small_bufs = [_cast_into_slot(w[n][0], chip_arr, F32, f"slot_{n}") for n in ("conv_w", "b_gate")]
    w_in_buf, conv_w_buf, b_gate_buf = _allgather_inplace([bufs["w_in"]], small_bufs, "allgather_w_in")
    wts = dict(w_in=w_in_buf.reshape(N_CHIPS, -1, w_in_buf.shape[3]),
               conv_w=conv_w_buf.transpose(1, 0, 2).reshape(CONV_WIDTH, -1),
               b_gate=b_gate_buf.transpose(1, 0, 2).reshape(2, 1, -1),
               norm1_w=norm1_w, q_norm_w=q_norm_w, k_norm_w=k_norm_w, conv_b=conv_b, conv_ln_w=conv_ln_w,
               conv_ln_b=conv_ln_b, norm2_w=norm2_w)

    loss, grad_x, g, reduced, by_chip_w_in = _forward_backward(
        x[0], positions.reshape(-1, 1), loss_target[0], wts, [bufs[n] for n in LATE_GATHER], pos_arr)
    grads = {n: b.reshape(-1, b.shape[2]) for n, b in reduced.items()}
    half_w_in = _sum_chips(by_chip_w_in, pos_arr, "grads_chip_sum_w_in")

    small_parts = [loss] + [g[n] for n in SMALL]
    small_shapes = [p.shape for p in small_parts]
    summed, (shard_w_in,) = _small_allreduce(_pack_small(small_parts), "small_allreduce",
                                             rider=_pair_gather_rider([half_w_in]))
    grads["w_in"] = shard_w_in.reshape(-1, shard_w_in.shape[2])
    reduced = _unpack_small(summed, small_shapes)
    loss_total = reduced[0].reshape(())
    for n, r in zip(SMALL, reduced[1:]):
        grads[n] = r
    ch_shard = conv_w.shape[2]
    grads["conv_w"] = lax.dynamic_slice_in_dim(grads["conv_w"], chip * ch_shard, ch_shard, axis=1)
    d_shard = b_gate.shape[2]
    grads["b_gate"] = lax.dynamic_slice_in_dim(grads["b_gate"], chip * d_shard, d_shard, axis=1)

    delta, new_m, new_v = {}, {}, {}
    for n in BIG:
        grads[n], delta[n], new_m[n], new_v[n] = _adamw(w[n][0], grads[n], m[n][0], v[n][0], f"adamw_{n}")
    flat2 = lambda a: a.reshape(-1, a.shape[-1])
    d_s, m_s, v_s = _adamw_small([flat2(w[n]) for n in SMALL], [flat2(grads[n]) for n in SMALL],
                                 [flat2(m[n]) for n in SMALL], [flat2(v[n]) for n in SMALL], "adamw_small")
    for i, n in enumerate(SMALL):
        delta[n], new_m[n], new_v[n] = d_s[i], m_s[i], v_s[i]

    shaped = lambda d, n: d[n].reshape(w[n].shape)
    return (loss_total, grad_x[None], *[shaped(grads, n) for n in ORDER], *[shaped(delta, n) for n in ORDER],
            *[shaped(new_m, n) for n in ORDER], *[shaped(new_v, n) for n in ORDER])
```

```python
import functools

import numpy as np
import jax
import jax.numpy as jnp
from jax import lax
from jax.experimental import pallas as pl
from jax.experimental.pallas import tpu as pltpu

F32 = jnp.float32
BF16 = jnp.bfloat16
MESH = pl.DeviceIdType.MESH
ANY = pl.BlockSpec(memory_space=pl.ANY)

HEAD_DIM = 64
N_SLOT_HEADS = 8
DILATIONS = (1, 4, 16)
HALF_SPAN = 64
ROPE_THETA = 500000.0
ROT_DIM = 16
CONV_WIDTH = 31
EPS = 1e-6
NEG_INF = -1e30
ADAM_LR, ADAM_B1, ADAM_B2, ADAM_EPS, ADAM_WD, ADAM_STEP = 0.001, 0.9, 0.999, 1e-08, 0.01, 10

LANES = 128
QBLK = 128
KWIN = QBLK + 2 * HALF_SPAN
VMEM_LIMIT = 48 * 1024 * 1024
N_CHIPS = 4


def _params(**kw):
    return pltpu.CompilerParams(vmem_limit_bytes=VMEM_LIMIT, **kw)


class _Rider:
    def __init__(self, operands, out_shapes, aliases, scratch, start, wait):
        self.operands, self.out_shapes, self.aliases = list(operands), list(out_shapes), dict(aliases)
        self.scratch, self.start, self.wait = list(scratch), start, wait


def _pallas(body, *, name, grid, in_specs, out_specs, out_shape, operands, scratch_shapes=(), aliases=None,
            rider=None):
    single = not isinstance(out_specs, (list, tuple))
    out_specs_l = [out_specs] if single else list(out_specs)
    out_shape_l = [out_shape] if single else list(out_shape)
    aliases = dict(aliases or {})
    if rider is None:
        res = pl.pallas_call(
            body, name=name, grid=grid, in_specs=list(in_specs), out_specs=out_specs_l, out_shape=out_shape_l,
            scratch_shapes=list(scratch_shapes), input_output_aliases=aliases, compiler_params=_params(),
        )(*operands)
        return res[0] if single else res
    n_in, n_rin = len(in_specs), len(rider.operands)
    n_out, n_rout = len(out_specs_l), len(rider.out_shapes)
    n_sc = len(scratch_shapes)

    def wrapped(*refs):
        main_in, r_in = refs[:n_in], refs[n_in:n_in + n_rin]
        o0 = n_in + n_rin
        main_out, r_out = refs[o0:o0 + n_out], refs[o0 + n_out:o0 + n_out + n_rout]
        s0 = o0 + n_out + n_rout
        main_sc, r_sc = refs[s0:s0 + n_sc], refs[s0 + n_sc:]
        ids = [pl.program_id(d) for d in range(len(grid))]
        first = functools.reduce(jnp.logical_and, [i == 0 for i in ids])
        last = functools.reduce(jnp.logical_and, [i == n - 1 for i, n in zip(ids, grid)])

        @pl.when(first)
        def _():
            rider.start(r_in, r_out, r_sc)

        body(*main_in, *main_out, *main_sc)

        @pl.when(last)
        def _():
            rider.wait(r_in, r_out, r_sc)

    for src, dst in rider.aliases.items():
        aliases[n_in + src] = n_out + dst
    res = pl.pallas_call(
        wrapped, name=name, grid=grid, in_specs=list(in_specs) + [ANY] * n_rin,
        out_specs=out_specs_l + [ANY] * n_rout, out_shape=out_shape_l + rider.out_shapes,
        scratch_shapes=list(scratch_shapes) + rider.scratch, input_output_aliases=aliases,
        compiler_params=_params(),
    )(*operands, *rider.operands)
    main = res[:n_out]
    return (main[0] if single else main), res[n_out:]


def _matmul(a, b, *, mode, tm, tn, tk, out_dtype, name, b_blocked=False,
            out_blocked=None, rider=None):
    a_shape = a.shape
    if mode == "nn":
        m_dim, k_dim = a_shape
        n_dim = b.shape[0] * b.shape[2] if b_blocked else b.shape[1]
        rows, cols, red = m_dim, n_dim, k_dim
    elif mode == "nt":
        m_dim, n_dim = a_shape
        k_dim = b.shape[1] if b_blocked else b.shape[0]
        rows, cols, red = m_dim, k_dim, n_dim
    else:
        m_dim, k_dim = a_shape
        n_dim = b.shape[1]
        rows, cols, red = k_dim, n_dim, m_dim
    assert rows % tm == 0 and cols % tn == 0 and red % tk == 0, (name, rows, cols, red)
    ni, nj, nk = rows // tm, cols // tn, red // tk

    if mode == "nn":
        a_spec = pl.BlockSpec((tm, tk), lambda i, j, k: (i, k))
        if b_blocked:
            per = b.shape[2] // tn
            b_spec = pl.BlockSpec((None, tk, tn), lambda i, j, k: (j // per, k, j % per))
        else:
            b_spec = pl.BlockSpec((tk, tn), lambda i, j, k: (k, j))
        dims = (((1,), (0,)), ((), ()))
    elif mode == "nt":
        a_spec = pl.BlockSpec((tm, tk), lambda i, j, k: (i, k))
        if b_blocked:
            per = b.shape[2] // tk
            b_spec = pl.BlockSpec((None, tn, tk), lambda i, j, k: (k // per, j, k % per))
        else:
            b_spec = pl.BlockSpec((tn, tk), lambda i, j, k: (j, k))
        dims = (((1,), (1,)), ((), ()))
    else:
        a_spec = pl.BlockSpec((tk, tm), lambda i, j, k: (k, i))
        b_spec = pl.BlockSpec((tk, tn), lambda i, j, k: (k, j))
        dims = (((0,), (0,)), ((), ()))

    if out_blocked:
        per_o = (cols // out_blocked) // tn
        out_spec = pl.BlockSpec((None, tm, tn), lambda i, j, k: (j // per_o, i, j % per_o))
        out_shape = jax.ShapeDtypeStruct((out_blocked, rows, cols // out_blocked), out_dtype)
    else:
        out_spec = pl.BlockSpec((tm, tn), lambda i, j, k: (i, j))
        out_shape = jax.ShapeDtypeStruct((rows, cols), out_dtype)

    def body(a_ref, b_ref, o_ref, *acc):
        prod = lax.dot_general(a_ref[...], b_ref[...], dims, preferred_element_type=F32)
        if nk == 1:
            o_ref[...] = prod.astype(out_dtype)
        else:
            acc_ref, = acc
            k = pl.program_id(2)

            @pl.when(k == 0)
            def _():
                acc_ref[...] = prod

            @pl.when(k > 0)
            def _():
                acc_ref[...] += prod

            @pl.when(k == nk - 1)
            def _():
                o_ref[...] = acc_ref[...].astype(out_dtype)

    scratch = [pltpu.VMEM((tm, tn), F32)] if nk > 1 else []
    return _pallas(body, name=name, grid=(ni, nj, nk), in_specs=[a_spec, b_spec], out_specs=out_spec,
                   out_shape=out_shape, operands=[a, b], scratch_shapes=scratch, rider=rider)


def _rmsnorm_fwd(x, w, name):
    s, d = x.shape
    tm = 256

    def body(x_ref, w_ref, o_ref):
        xv = x_ref[...]
        rstd = lax.rsqrt(jnp.mean(xv * xv, axis=-1, keepdims=True) + EPS)
        o_ref[...] = (xv * rstd * w_ref[...]).astype(BF16)

    return pl.pallas_call(
        body, name=name, grid=(s // tm,),
        in_specs=[pl.BlockSpec((tm, d), lambda i: (i, 0)), pl.BlockSpec((1, d), lambda i: (0, 0))],
        out_specs=pl.BlockSpec((tm, d), lambda i: (i, 0)),
        out_shape=jax.ShapeDtypeStruct((s, d), BF16), compiler_params=_params(),
    )(x, w)


def _rmsnorm_bwd(dh, x, w, dres, name, rider=None):
    s, d = x.shape
    tm = 256

    def body(dh_ref, x_ref, w_ref, dres_ref, dx_ref, dxb_ref, dw_ref):
        xv = x_ref[...]
        rstd = lax.rsqrt(jnp.mean(xv * xv, axis=-1, keepdims=True) + EPS)
        xhat = xv * rstd
        dhv = dh_ref[...]
        g = dhv * w_ref[...]
        dx = rstd * (g - xhat * jnp.mean(g * xhat, axis=-1, keepdims=True)) + dres_ref[...]
        dx_ref[...] = dx
        dxb_ref[...] = dx.astype(BF16)
        part = jnp.sum(dhv * xhat, axis=0, keepdims=True)

        @pl.when(pl.program_id(0) == 0)
        def _():
            dw_ref[...] = part

        @pl.when(pl.program_id(0) > 0)
        def _():
            dw_ref[...] += part

    row = pl.BlockSpec((tm, d), lambda i: (i, 0))
    vec = pl.BlockSpec((1, d), lambda i: (0, 0))
    return _pallas(
        body, name=name, grid=(s // tm,), in_specs=[row, row, vec, row], out_specs=[row, row, vec],
        out_shape=[jax.ShapeDtypeStruct((s, d), F32), jax.ShapeDtypeStruct((s, d), BF16),
                   jax.ShapeDtypeStruct((1, d), F32)],
        operands=[dh, x, w, dres], rider=rider)


def _rope_consts():
    lane = np.arange(LANES)
    in_head = lane % HEAD_DIM
    inv_freq = ROPE_THETA ** (-jnp.arange(0, ROT_DIM, 2, dtype=F32) / ROT_DIM)
    invf = jnp.where(jnp.asarray(in_head < ROT_DIM), jnp.tile(inv_freq, LANES // (ROT_DIM // 2)), 0.0)
    m_a = np.where(in_head < ROT_DIM // 2, -1.0, 0.0).astype(np.float32)
    m_b = np.where((in_head >= ROT_DIM // 2) & (in_head < ROT_DIM), 1.0, 0.0).astype(np.float32)
    block_diag = (lane[:, None] // HEAD_DIM == lane[None, :] // HEAD_DIM).astype(np.float32)
    return (invf.reshape(1, LANES).astype(F32), jnp.asarray(m_a).reshape(1, LANES),
            jnp.asarray(m_b).reshape(1, LANES), jnp.asarray(block_diag, dtype=BF16))


def _head_sums(v, bd):
    hi = v.astype(BF16)
    lo = (v - hi.astype(F32)).astype(BF16)
    return jnp.dot(hi, bd, preferred_element_type=F32) + jnp.dot(lo, bd, preferred_element_type=F32)


def _qk_fwd(proj, pos_col, qw2, kw2, consts, name, rider=None):
    s = proj.shape[0]
    width = 3 * N_SLOT_HEADS * HEAD_DIM
    tm = 128
    invf, m_a, m_b, bd = consts
    scale = HEAD_DIM ** -0.5

    def body(q_ref, k_ref, pos_ref, qw_ref, kw_ref, invf_ref, ma_ref, mb_ref, bd_ref, qo_ref, ko_ref):
        ang = pos_ref[...].astype(F32) * invf_ref[...]
        cos = jnp.cos(ang)
        sin = jnp.sin(ang)
        s_a = sin * ma_ref[...]
        s_b = sin * mb_ref[...]
        bdv = bd_ref[...]
        for src, w_ref, dst, sc in ((q_ref, qw_ref, qo_ref, scale), (k_ref, kw_ref, ko_ref, 1.0)):
            for cb in range(width // LANES):
                cols = slice(cb * LANES, (cb + 1) * LANES)
                t = src[:, cols]
                rstd = lax.rsqrt(_head_sums(t * t, bdv) * (1.0 / HEAD_DIM) + EPS)
                y = t * rstd * w_ref[...]
                r = y * cos + pltpu.roll(y, LANES - 8, axis=1) * s_a + pltpu.roll(y, 8, axis=1) * s_b
                dst[:, cols] = r * sc if sc != 1.0 else r

    vec = pl.BlockSpec((1, LANES), lambda i: (0, 0))
    return _pallas(
        body, name=name, grid=(s // tm,),
        in_specs=[pl.BlockSpec((tm, width), lambda i: (i, 0)), pl.BlockSpec((tm, width), lambda i: (i, 1)),
                  pl.BlockSpec((tm, 1), lambda i: (i, 0)), vec, vec, vec, vec, vec,
                  pl.BlockSpec((LANES, LANES), lambda i: (0, 0))],
        out_specs=[pl.BlockSpec((tm, width), lambda i: (i, 0))] * 2,
        out_shape=[jax.ShapeDtypeStruct((s, width), F32)] * 2,
        operands=[proj, proj, pos_col, qw2, kw2, invf, m_a, m_b, bd], rider=rider)


def _qk_bwd(dqn, dkn, dv, da, db, dgl, proj, pos_col, qw2, kw2, consts, name, rider=None):
    s = proj.shape[0]
    width = 3 * N_SLOT_HEADS * HEAD_DIM
    ch = da.shape[1]
    gate_w = dgl.shape[1]
    out_w = 3 * width + 2 * ch + gate_w
    assert out_w == proj.shape[1]
    tm = 128
    invf, m_a, m_b, bd = consts
    scale = HEAD_DIM ** -0.5

    def body(dq_ref, dk_ref, dv_ref, da_ref, db_ref, dgl_ref, q_ref, k_ref, pos_ref, qw_ref, kw_ref,
             invf_ref, ma_ref, mb_ref, bd_ref, out_ref, dqw_ref, dkw_ref):
        ang = pos_ref[...].astype(F32) * invf_ref[...]
        cos = jnp.cos(ang)
        sin = jnp.sin(ang)
        s_a = sin * ma_ref[...]
        s_b = sin * mb_ref[...]
        bdv = bd_ref[...]
        first = pl.program_id(0) == 0
        for src, dsrc, w_ref, col0, dw_ref, sc in ((q_ref, dq_ref, qw_ref, 0, dqw_ref, scale),
                                                   (k_ref, dk_ref, kw_ref, width, dkw_ref, 1.0)):
            dw_acc = jnp.zeros((1, LANES), F32)
            for cb in range(width // LANES):
                cols = slice(cb * LANES, (cb + 1) * LANES)
                t = src[:, cols]
                dr = dsrc[:, cols]
                if sc != 1.0:
                    dr = dr * sc
                dy = dr * cos + pltpu.roll(dr * s_a, 8, axis=1) + pltpu.roll(dr * s_b, LANES - 8, axis=1)
                rstd = lax.rsqrt(_head_sums(t * t, bdv) * (1.0 / HEAD_DIM) + EPS)
                xhat = t * rstd
                g = dy * w_ref[...]
                dt = rstd * (g - xhat * (_head_sums(g * xhat, bdv) * (1.0 / HEAD_DIM)))
                out_ref[:, col0 + cb * LANES: col0 + (cb + 1) * LANES] = dt.astype(BF16)
                dw_acc = dw_acc + jnp.sum(dy * xhat, axis=0, keepdims=True)
            dw_acc = dw_acc + pltpu.roll(dw_acc, HEAD_DIM, axis=1)

            @pl.when(first)
            def _(dw_ref=dw_ref, dw_acc=dw_acc):
                dw_ref[...] = dw_acc

            @pl.when(jnp.logical_not(first))
            def _(dw_ref=dw_ref, dw_acc=dw_acc):
                dw_ref[...] += dw_acc
        out_ref[:, 2 * width: 3 * width] = dv_ref[...].astype(BF16)
        out_ref[:, 3 * width: 3 * width + ch] = da_ref[...]
        out_ref[:, 3 * width + ch: 3 * width + 2 * ch] = db_ref[...]
        out_ref[:, 3 * width + 2 * ch: out_w] = dgl_ref[...]

    vec = pl.BlockSpec((1, LANES), lambda i: (0, 0))
    blk = lambda c: pl.BlockSpec((tm, width), lambda i: (i, c))
    cblk = pl.BlockSpec((tm, ch), lambda i: (i, 0))
    return _pallas(
        body, name=name, grid=(s // tm,),
        in_specs=[blk(0), blk(0), blk(0), cblk, cblk, pl.BlockSpec((tm, gate_w), lambda i: (i, 0)),
                  blk(0), blk(1), pl.BlockSpec((tm, 1), lambda i: (i, 0)), vec, vec, vec, vec, vec,
                  pl.BlockSpec((LANES, LANES), lambda i: (0, 0))],
        out_specs=[pl.BlockSpec((tm, out_w), lambda i: (i, 0)), vec, vec],
        out_shape=[jax.ShapeDtypeStruct((s, out_w), BF16)] + [jax.ShapeDtypeStruct((1, LANES), F32)] * 2,
        operands=[dqn, dkn, dv, da, db, dgl, proj, proj, pos_col, qw2, kw2, invf, m_a, m_b, bd],
        rider=rider)


def _row_chunks(n_rows, fn, chunk=256):
    def step(i, c):
        fn(pl.ds(pl.multiple_of(i * chunk, chunk), chunk))
        return c
    lax.fori_loop(0, n_rows // chunk, step, 0)


def _to_residue_major(dst, src, s, d, dst_off=0, cast=None):
    seq = s // d
    for r in range(d):
        v = src[...] if d == 1 else src[pl.ds(r, seq, stride=d), :]
        dst[dst_off + r * seq: dst_off + (r + 1) * seq, :] = v if cast is None else v.astype(cast)


def _from_residue_major(dst, src, s, d, src_off=0):
    seq = s // d
    for r in range(d):
        v = src[src_off + r * seq: src_off + (r + 1) * seq, :]
        if d == 1:
            dst[...] = v
        else:
            dst[pl.ds(r, seq, stride=d), :] = v


def _band_bias():
    qi = lax.broadcasted_iota(jnp.int32, (QBLK, KWIN), 0)
    kj = lax.broadcasted_iota(jnp.int32, (QBLK, KWIN), 1)
    return jnp.where(jnp.abs(kj - HALF_SPAN - qi) <= HALF_SPAN, 0.0, NEG_INF).astype(F32)


def _range_bias(base, seq):
    kj = lax.broadcasted_iota(jnp.int32, (1, KWIN), 1)
    lo = (base & -seq) - base + HALF_SPAN
    return jnp.where((kj >= lo) & (kj < lo + seq), 0.0, NEG_INF).astype(F32)


def _block_base(b):
    return b * QBLK if isinstance(b, int) else pl.multiple_of(b * QBLK, QBLK)


def _attn_fwd(qn, kn, proj, name, rider=None):
    s = qn.shape[0]
    n_pairs = N_SLOT_HEADS * HEAD_DIM // LANES
    v_col0 = 2 * qn.shape[1] // LANES
    nt_dims = (((1,), (1,)), ((), ()))

    def body(q_ref, k_ref, v_ref, attn_ref, lse_ref, attn_b_ref, q_rm, k_rm, v_rm, acc_rm, m_rm, l_rm,
             acc_p, m_p, l_p, m_run, l_run, acc_run, band, s_buf, m_buf):
        g = pl.program_id(1)
        zpad = jnp.zeros((HALF_SPAN, LANES), BF16)
        k_rm[0:HALF_SPAN, :] = zpad
        k_rm[s + HALF_SPAN: s + 2 * HALF_SPAN, :] = zpad
        v_rm[0:HALF_SPAN, 0:LANES] = zpad
        v_rm[s + HALF_SPAN: s + 2 * HALF_SPAN, 0:LANES] = zpad

        def ones_rows(rows):
            v_rm[pl.ds(rows.start, rows.size), LANES:2 * LANES] = jnp.ones((rows.size, LANES), BF16)

        _row_chunks(s + 2 * HALF_SPAN, ones_rows, chunk=2 * HALF_SPAN)
        band[...] = _band_bias()
        lane = lax.broadcasted_iota(jnp.int32, (QBLK, LANES), 1)
        low = lane < HEAD_DIM
        n_blk = s // QBLK

        for gi, d in enumerate(DILATIONS):
            @pl.when(g == gi)
            def _(gi=gi, d=d):
                seq = s // d
                _to_residue_major(q_rm, q_ref, s, d, cast=BF16)
                _to_residue_major(k_rm, k_ref, s, d, dst_off=HALF_SPAN, cast=BF16)
                _to_residue_major(v_rm.at[:, 0:LANES], v_ref, s, d, dst_off=HALF_SPAN, cast=BF16)

                def scores(b, slot):
                    base = _block_base(b)
                    q = q_rm[pl.ds(base, QBLK), :]
                    zero = jnp.zeros_like(q)
                    q2 = jnp.concatenate([jnp.where(low, q, zero), jnp.where(low, zero, q)], axis=0)
                    sc = lax.dot_general(q2, k_rm[pl.ds(base, KWIN), :], nt_dims, preferred_element_type=F32)
                    bias = band[...] + _range_bias(base, seq)
                    for hh in range(2):
                        rows = slice(hh * QBLK, (hh + 1) * QBLK)
                        sh = sc[rows, :] + bias
                        s_buf[slot, rows, :] = sh
                        m_buf[slot, rows, :] = jnp.broadcast_to(jnp.max(sh, axis=-1, keepdims=True), (QBLK, LANES))

                def outputs(b, slot):
                    base = _block_base(b)
                    sv = s_buf[slot]
                    mb = m_buf[slot]
                    p = jnp.exp(jnp.concatenate([sv[:, 0:LANES] - mb, sv[:, LANES:2 * LANES] - mb], axis=1))
                    pv = jnp.dot(p.astype(BF16), v_rm[pl.ds(base, KWIN), :], preferred_element_type=F32)
                    rows = pl.ds(base, QBLK)
                    acc_rm[rows, :] = jnp.where(low, pv[0:QBLK, 0:LANES], pv[QBLK:2 * QBLK, 0:LANES])
                    l_rm[rows, :] = jnp.where(low, pv[0:QBLK, LANES:2 * LANES], pv[QBLK:2 * QBLK, LANES:2 * LANES])
                    m_rm[rows, :] = jnp.where(low, mb[0:QBLK, :], mb[QBLK:2 * QBLK, :])

                scores(0, 0)

                def pair(i, carry):
                    b = 2 * i
                    outputs(b, 0)
                    scores(b + 1, 1)
                    outputs(b + 1, 1)
                    scores(b + 2, 0)
                    return carry

                lax.fori_loop(0, n_blk // 2 - 1, pair, 0)
                outputs(n_blk - 2, 0)
                scores(n_blk - 1, 1)
                outputs(n_blk - 1, 1)
                if d == 1:
                    src = (acc_rm, m_rm, l_rm)
                else:
                    for dst_, src_ in ((acc_p, acc_rm), (m_p, m_rm), (l_p, l_rm)):
                        _from_residue_major(dst_, src_, s, d)
                    src = (acc_p, m_p, l_p)

                def combine(rows):
                    a_g, m_g, l_g = src[0][rows, :], src[1][rows, :], src[2][rows, :]
                    if gi == 0:
                        m_new, l_new, a_new = m_g, l_g, a_g
                    else:
                        m_old = m_run[rows, :]
                        m_new = jnp.maximum(m_old, m_g)
                        w_old = jnp.exp(m_old - m_new)
                        w_g = jnp.exp(m_g - m_new)
                        l_new = l_run[rows, :] * w_old + l_g * w_g
                        a_new = acc_run[rows, :] * w_old + a_g * w_g
                    if gi == len(DILATIONS) - 1:
                        out = a_new / l_new
                        attn_ref[rows, :] = out
                        attn_b_ref[rows, :] = out.astype(BF16)
                        lse_ref[rows, :] = m_new + jnp.log(l_new)
                    else:
                        m_run[rows, :] = m_new
                        l_run[rows, :] = l_new
                        acc_run[rows, :] = a_new

                _row_chunks(s, combine)

    qk_spec = pl.BlockSpec((s, LANES), lambda hp, g: (0, g * n_pairs + hp))
    v_spec = pl.BlockSpec((s, LANES), lambda hp, g: (0, v_col0 + g * n_pairs + hp))
    o_spec = pl.BlockSpec((s, LANES), lambda hp, g: (0, hp))
    f32buf = pltpu.VMEM((s, LANES), F32)
    return _pallas(
        body, name=name, grid=(n_pairs, len(DILATIONS)), in_specs=[qk_spec, qk_spec, v_spec],
        out_specs=[o_spec, o_spec, o_spec],
        out_shape=[jax.ShapeDtypeStruct((s, n_pairs * LANES), F32)] * 2
        + [jax.ShapeDtypeStruct((s, n_pairs * LANES), BF16)],
        operands=[qn, kn, proj],
        scratch_shapes=[pltpu.VMEM((s, LANES), BF16), pltpu.VMEM((s + 2 * HALF_SPAN, LANES), BF16),
                        pltpu.VMEM((s + 2 * HALF_SPAN, 2 * LANES), BF16)] + [f32buf] * 9
        + [pltpu.VMEM((QBLK, KWIN), F32), pltpu.VMEM((2, 2 * QBLK, KWIN), F32),
           pltpu.VMEM((2, 2 * QBLK, LANES), F32)],
        rider=rider)


def _attn_bwd(qn, kn, proj, dattn, attn, lse, bd, name, rider=None):
    s = qn.shape[0]
    n_pairs = N_SLOT_HEADS * HEAD_DIM // LANES
    v_col0 = 2 * qn.shape[1] // LANES
    nt_dims = (((1,), (1,)), ((), ()))
    tn_dims = (((0,), (0,)), ((), ()))
    spad = s + 2 * HALF_SPAN

    def body(q_ref, k_ref, v_ref, do_ref, o_ref, lse_ref, bd_ref, dq_ref, dk_ref, dv_ref,
             q_rm, k_rm, v_rm, do_rm, lse0_rm, lse1_rm, dd0_rm, dd1_rm, dq_rm, dk_rm, dv_rm,
             lse0_p, lse1_p, dd0_p, dd1_p, band, p_buf, ds_buf):
        g = pl.program_id(1)
        zpad = jnp.zeros((HALF_SPAN, LANES), BF16)
        for buf in (k_rm, v_rm):
            buf[0:HALF_SPAN, :] = zpad
            buf[s + HALF_SPAN: spad, :] = zpad
        zf = jnp.zeros((HALF_SPAN, LANES), F32)
        for buf in (dk_rm, dv_rm):
            buf[0:HALF_SPAN, :] = zf
            buf[s + HALF_SPAN: spad, :] = zf
        band[...] = _band_bias()

        def clear(rows):
            z = jnp.zeros((rows.size, LANES), F32)
            dk_rm[pl.ds(rows.start + HALF_SPAN, rows.size), :] = z
            dv_rm[pl.ds(rows.start + HALF_SPAN, rows.size), :] = z

        _row_chunks(s, clear)

        def prepare(rows):
            lo = lax.broadcasted_iota(jnp.int32, (rows.size, LANES), 1) < HEAD_DIM
            dsum = _head_sums(do_ref[rows, :] * o_ref[rows, :], bd_ref[...])
            dswap = pltpu.roll(dsum, HEAD_DIM, axis=1)
            dd0_p[rows, :] = jnp.where(lo, dsum, dswap)
            dd1_p[rows, :] = jnp.where(lo, dswap, dsum)
            lv = lse_ref[rows, :]
            lswap = pltpu.roll(lv, HEAD_DIM, axis=1)
            lse0_p[rows, :] = jnp.where(lo, lv, lswap)
            lse1_p[rows, :] = jnp.where(lo, lswap, lv)

        @pl.when(g == 0)
        def _():
            _row_chunks(s, prepare)
        lane = lax.broadcasted_iota(jnp.int32, (QBLK, LANES), 1)
        low = lane < HEAD_DIM
        n_blk = s // QBLK

        def stacked(ref, rows):
            val = ref[rows, :]
            zero = jnp.zeros_like(val)
            return jnp.concatenate([jnp.where(low, val, zero), jnp.where(low, zero, val)], axis=0)

        for gi, d in enumerate(DILATIONS):
            @pl.when(g == gi)
            def _(d=d):
                seq = s // d
                _to_residue_major(q_rm, q_ref, s, d, cast=BF16)
                _to_residue_major(k_rm, k_ref, s, d, dst_off=HALF_SPAN, cast=BF16)
                _to_residue_major(v_rm, v_ref, s, d, dst_off=HALF_SPAN, cast=BF16)
                _to_residue_major(do_rm, do_ref, s, d, cast=BF16)
                for dst_, src_ in ((lse0_rm, lse0_p), (lse1_rm, lse1_p), (dd0_rm, dd0_p), (dd1_rm, dd1_p)):
                    _to_residue_major(dst_, src_, s, d)

                def scores(b, slot):
                    base = _block_base(b)
                    rows = pl.ds(base, QBLK)
                    win = pl.ds(base, KWIN)
                    sc = lax.dot_general(stacked(q_rm, rows), k_rm[win, :], nt_dims, preferred_element_type=F32)
                    dp = lax.dot_general(stacked(do_rm, rows), v_rm[win, :], nt_dims, preferred_element_type=F32)
                    bias = band[...] + _range_bias(base, seq)
                    for hh, (lse_r, dd_r) in enumerate(((lse0_rm, dd0_rm), (lse1_rm, dd1_rm))):
                        r = slice(hh * QBLK, (hh + 1) * QBLK)
                        lse_h = lse_r[rows, :]
                        dd_h = dd_r[rows, :]
                        sh = sc[r, :] + bias
                        p = jnp.exp(jnp.concatenate([sh[:, 0:LANES] - lse_h, sh[:, LANES:KWIN] - lse_h], axis=1))
                        dph = dp[r, :]
                        ds = p * jnp.concatenate([dph[:, 0:LANES] - dd_h, dph[:, LANES:KWIN] - dd_h], axis=1)
                        p_buf[slot, r, :] = p.astype(BF16)
                        ds_buf[slot, r, :] = ds.astype(BF16)

                def grads(b, slot):
                    base = _block_base(b)
                    rows = pl.ds(base, QBLK)
                    win = pl.ds(base, KWIN)
                    p = p_buf[slot]
                    ds = ds_buf[slot]
                    dq2 = jnp.dot(ds, k_rm[win, :], preferred_element_type=F32)
                    dq_rm[rows, :] = jnp.where(low, dq2[0:QBLK, :], dq2[QBLK:2 * QBLK, :])
                    dk_rm[win, :] += lax.dot_general(ds, stacked(q_rm, rows), tn_dims, preferred_element_type=F32)
                    dv_rm[win, :] += lax.dot_general(p, stacked(do_rm, rows), tn_dims, preferred_element_type=F32)

                scores(0, 0)

                def pair(i, carry):
                    b = 2 * i
                    grads(b, 0)
                    scores(b + 1, 1)
                    grads(b + 1, 1)
                    scores(b + 2, 0)
                    return carry

                lax.fori_loop(0, n_blk // 2 - 1, pair, 0)
                grads(n_blk - 2, 0)
                scores(n_blk - 1, 1)
                grads(n_blk - 1, 1)
                _from_residue_major(dq_ref, dq_rm, s, d)
                _from_residue_major(dk_ref, dk_rm, s, d, src_off=HALF_SPAN)
                _from_residue_major(dv_ref, dv_rm, s, d, src_off=HALF_SPAN)

    qk_spec = pl.BlockSpec((s, LANES), lambda hp, g: (0, g * n_pairs + hp))
    v_spec = pl.BlockSpec((s, LANES), lambda hp, g: (0, v_col0 + g * n_pairs + hp))
    o_spec = pl.BlockSpec((s, LANES), lambda hp, g: (0, hp))
    width = qn.shape[1]
    f32buf = pltpu.VMEM((s, LANES), F32)
    f32pad = pltpu.VMEM((spad, LANES), F32)
    return _pallas(
        body, name=name, grid=(n_pairs, len(DILATIONS)),
        in_specs=[qk_spec, qk_spec, v_spec, o_spec, o_spec, o_spec,
                  pl.BlockSpec((LANES, LANES), lambda hp, g: (0, 0))],
        out_specs=[qk_spec, qk_spec, qk_spec],
        out_shape=[jax.ShapeDtypeStruct((s, width), F32)] * 3,
        operands=[qn, kn, proj, dattn, attn, lse, bd],
        scratch_shapes=[pltpu.VMEM((s, LANES), BF16), pltpu.VMEM((spad, LANES), BF16),
                        pltpu.VMEM((spad, LANES), BF16), pltpu.VMEM((s, LANES), BF16),
                        f32buf, f32buf, f32buf, f32buf, f32buf, f32pad, f32pad,
                        f32buf, f32buf, f32buf, f32buf, pltpu.VMEM((QBLK, KWIN), F32),
                        pltpu.VMEM((2, 2 * QBLK, KWIN), BF16), pltpu.VMEM((2, 2 * QBLK, KWIN), BF16)],
        rider=rider)


CONV_PAD = 16


def _conv_fwd(proj, conv_w, conv_b, col0, name, rider=None):
    s = proj.shape[0]
    ch = conv_w.shape[1]
    nblk = ch // LANES
    a0 = col0 // LANES
    tr = 256
    shift = CONV_PAD - (CONV_WIDTH - 1) // 2

    def body(a_ref, b_ref, w_ref, bias_ref, u0_ref, uc_ref, pad):
        z = jnp.zeros((CONV_PAD, LANES), F32)
        pad[0:CONV_PAD, :] = z
        pad[s + CONV_PAD: s + 2 * CONV_PAD, :] = z

        def glu(rows):
            u0 = a_ref[rows, :] * jax.nn.sigmoid(b_ref[rows, :])
            u0_ref[rows, :] = u0
            pad[pl.ds(rows.start + CONV_PAD, rows.size), :] = u0

        _row_chunks(s, glu)
        for t in range(0, s, tr):
            acc = jnp.broadcast_to(bias_ref[...], (tr, LANES))
            for k in range(CONV_WIDTH):
                acc = acc + w_ref[k:k + 1, :] * pad[t + k + shift: t + k + shift + tr, :]
            uc_ref[t:t + tr, :] = acc

    return _pallas(
        body, name=name, grid=(nblk,),
        in_specs=[pl.BlockSpec((s, LANES), lambda c: (0, a0 + c)),
                  pl.BlockSpec((s, LANES), lambda c: (0, a0 + nblk + c)),
                  pl.BlockSpec((CONV_WIDTH, LANES), lambda c: (0, c)),
                  pl.BlockSpec((1, LANES), lambda c: (0, c))],
        out_specs=[pl.BlockSpec((s, LANES), lambda c: (0, c))] * 2,
        out_shape=[jax.ShapeDtypeStruct((s, ch), F32)] * 2, operands=[proj, proj, conv_w, conv_b],
        scratch_shapes=[pltpu.VMEM((s + 2 * CONV_PAD, LANES), F32)], rider=rider)


def _ln_silu_fwd(uc, ln_w, ln_b, name):
    s, ch = uc.shape
    tm = 256

    def body(u_ref, w_ref, b_ref, o_ref):
        u = u_ref[...]
        mu = jnp.mean(u, axis=-1, keepdims=True)
        xc = u - mu
        rstd = lax.rsqrt(jnp.mean(xc * xc, axis=-1, keepdims=True) + EPS)
        z = xc * rstd * w_ref[...] + b_ref[...]
        o_ref[...] = (z * jax.nn.sigmoid(z)).astype(BF16)

    row = pl.BlockSpec((tm, ch), lambda i: (i, 0))
    vec = pl.BlockSpec((1, ch), lambda i: (0, 0))
    return pl.pallas_call(
        body, name=name, grid=(s // tm,), in_specs=[row, vec, vec], out_specs=row,
        out_shape=jax.ShapeDtypeStruct((s, ch), BF16), compiler_params=_params(),
    )(uc, ln_w, ln_b)


def _ln_silu_bwd(du3, uc, ln_w, ln_b, name):
    s, ch = uc.shape
    tm = 256

    def body(d_ref, u_ref, w_ref, b_ref, du_ref, dw_ref, db_ref):
        u = u_ref[...]
        mu = jnp.mean(u, axis=-1, keepdims=True)
        xc = u - mu
        rstd = lax.rsqrt(jnp.mean(xc * xc, axis=-1, keepdims=True) + EPS)
        xhat = xc * rstd
        z = xhat * w_ref[...] + b_ref[...]
        sg = jax.nn.sigmoid(z)
        dz = d_ref[...] * (sg * (1.0 + z * (1.0 - sg)))
        dxh = dz * w_ref[...]
        du_ref[...] = rstd * (dxh - jnp.mean(dxh, axis=-1, keepdims=True)
                              - xhat * jnp.mean(dxh * xhat, axis=-1, keepdims=True))
        pw = jnp.sum(dz * xhat, axis=0, keepdims=True)
        pb = jnp.sum(dz, axis=0, keepdims=True)
        first = pl.program_id(0) == 0

        @pl.when(first)
        def _():
            dw_ref[...] = pw
            db_ref[...] = pb

        @pl.when(jnp.logical_not(first))
        def _():
            dw_ref[...] += pw
            db_ref[...] += pb

    row = pl.BlockSpec((tm, ch), lambda i: (i, 0))
    vec = pl.BlockSpec((1, ch), lambda i: (0, 0))
    return pl.pallas_call(
        body, name=name, grid=(s // tm,), in_specs=[row, row, vec, vec], out_specs=[row, vec, vec],
        out_shape=[jax.ShapeDtypeStruct((s, ch), F32), jax.ShapeDtypeStruct((1, ch), F32),
                   jax.ShapeDtypeStruct((1, ch), F32)],
        compiler_params=_params(),
    )(du3, uc, ln_w, ln_b)


def _conv_bwd(duc, u0, proj, conv_w, col0, name, rider=None):
    s = proj.shape[0]
    ch = conv_w.shape[1]
    nblk = ch // LANES
    a0 = col0 // LANES
    tr = 256
    half = (CONV_WIDTH - 1) // 2
    shift = CONV_PAD - half

    def body(duc_ref, u0_ref, a_ref, b_ref, w_ref, da_ref, db_ref, dw_ref, dbias_ref, pad_d, pad_u):
        z = jnp.zeros((CONV_PAD, LANES), F32)
        for buf in (pad_d, pad_u):
            buf[0:CONV_PAD, :] = z
            buf[s + CONV_PAD: s + 2 * CONV_PAD, :] = z

        def fill(rows):
            dst = pl.ds(rows.start + CONV_PAD, rows.size)
            pad_d[dst, :] = duc_ref[rows, :]
            pad_u[dst, :] = u0_ref[rows, :]

        _row_chunks(s, fill)
        dw_acc = [jnp.zeros((8, LANES), F32) for _ in range(CONV_WIDTH)]
        dbias_acc = jnp.zeros((8, LANES), F32)
        for t in range(0, s, tr):
            d_t = duc_ref[t:t + tr, :]
            dbias_acc = dbias_acc + jnp.sum(d_t.reshape(tr // 8, 8, LANES), axis=0)
            du0 = jnp.zeros((tr, LANES), F32)
            for k in range(CONV_WIDTH):
                du0 = du0 + w_ref[k:k + 1, :] * pad_d[t - k + half + CONV_PAD: t - k + half + CONV_PAD + tr, :]
                prod = d_t * pad_u[t + k + shift: t + k + shift + tr, :]
                dw_acc[k] = dw_acc[k] + jnp.sum(prod.reshape(tr // 8, 8, LANES), axis=0)
            av = a_ref[t:t + tr, :]
            sg = jax.nn.sigmoid(b_ref[t:t + tr, :])
            da_ref[t:t + tr, :] = (du0 * sg).astype(BF16)
            db_ref[t:t + tr, :] = (du0 * av * sg * (1.0 - sg)).astype(BF16)
        for k in range(CONV_WIDTH):
            dw_ref[k:k + 1, :] = jnp.sum(dw_acc[k], axis=0, keepdims=True)
        dbias_ref[...] = jnp.sum(dbias_acc, axis=0, keepdims=True)

    col = lambda off: pl.BlockSpec((s, LANES), lambda c: (0, off + c))
    return _pallas(
        body, name=name, grid=(nblk,),
        in_specs=[col(0), col(0), col(a0), col(a0 + nblk),
                  pl.BlockSpec((CONV_WIDTH, LANES), lambda c: (0, c))],
        out_specs=[col(0), col(0), pl.BlockSpec((CONV_WIDTH, LANES), lambda c: (0, c)),
                   pl.BlockSpec((1, LANES), lambda c: (0, c))],
        out_shape=[jax.ShapeDtypeStruct((s, ch), BF16)] * 2
        + [jax.ShapeDtypeStruct((CONV_WIDTH, ch), F32), jax.ShapeDtypeStruct((1, ch), F32)],
        operands=[duc, u0, proj, proj, conv_w],
        scratch_shapes=[pltpu.VMEM((s + 2 * CONV_PAD, LANES), F32)] * 2, rider=rider)


GATE_BLK = 512


def _gate_fwd(proj, bg, y_a, y_b, col0, name):
    s, d = y_a.shape
    tm = 256
    g0 = col0 // GATE_BLK
    nb = d // GATE_BLK

    def body(ga_ref, gb_ref, ba_ref, bb_ref, ya_ref, yb_ref, o_ref):
        g_a = jax.nn.sigmoid(ga_ref[...] + ba_ref[...])
        g_b = jax.nn.sigmoid(gb_ref[...] + bb_ref[...])
        o_ref[...] = (g_a * ya_ref[...] + g_b * yb_ref[...]).astype(BF16)

    act = pl.BlockSpec((tm, GATE_BLK), lambda i, j: (i, j))
    return pl.pallas_call(
        body, name=name, grid=(s // tm, nb),
        in_specs=[pl.BlockSpec((tm, GATE_BLK), lambda i, j: (i, g0 + j)),
                  pl.BlockSpec((tm, GATE_BLK), lambda i, j: (i, g0 + nb + j)),
                  pl.BlockSpec((None, 1, GATE_BLK), lambda i, j: (0, 0, j)),
                  pl.BlockSpec((None, 1, GATE_BLK), lambda i, j: (1, 0, j)), act, act],
        out_specs=act, out_shape=jax.ShapeDtypeStruct((s, d), BF16), compiler_params=_params(),
    )(proj, proj, bg, bg, y_a, y_b)


def _gate_bwd(d_mixed, proj, bg, y_a, y_b, col0, name, rider=None):
    s, d = y_a.shape
    tm = 256
    half = d // 2
    assert col0 % half == 0
    c0 = col0 // half

    def body(dm_ref, a0_ref, a1_ref, b0_ref, b1_ref, bias_ref, ya_ref, yb_ref, dgl_ref, dya_ref, dyb_ref, db_ref):
        dm = dm_ref[...]
        parts = []
        for br, (lo_ref, hi_ref, y_ref, dy_ref) in enumerate(((a0_ref, a1_ref, ya_ref, dya_ref),
                                                              (b0_ref, b1_ref, yb_ref, dyb_ref))):
            logits = jnp.concatenate([lo_ref[...], hi_ref[...]], axis=1)
            gate = jax.nn.sigmoid(logits + bias_ref[br])
            dy_ref[...] = (dm * gate).astype(BF16)
            dgl = dm * y_ref[...] * gate * (1.0 - gate)
            dgl_ref[:, br * d:(br + 1) * d] = dgl.astype(BF16)
            parts.append(jnp.sum(dgl, axis=0, keepdims=True))
        part = jnp.concatenate(parts, axis=0)
        first = pl.program_id(0) == 0

        @pl.when(first)
        def _():
            db_ref[...] = part

        @pl.when(jnp.logical_not(first))
        def _():
            db_ref[...] += part

    row = pl.BlockSpec((tm, d), lambda i: (i, 0))
    logit_blk = lambda k: pl.BlockSpec((tm, half), functools.partial(lambda i, k: (i, c0 + k), k=k))
    return _pallas(
        body, name=name, grid=(s // tm,),
        in_specs=[row, logit_blk(0), logit_blk(1), logit_blk(2), logit_blk(3),
                  pl.BlockSpec((2, 1, d), lambda i: (0, 0, 0)), row, row],
        out_specs=[pl.BlockSpec((tm, 2 * d), lambda i: (i, 0)), row, row, pl.BlockSpec((2, d), lambda i: (0, 0))],
        out_shape=[jax.ShapeDtypeStruct((s, 2 * d), BF16), jax.ShapeDtypeStruct((s, d), BF16),
                   jax.ShapeDtypeStruct((s, d), BF16), jax.ShapeDtypeStruct((2, d), F32)],
        operands=[d_mixed, proj, proj, proj, proj, bg, y_a, y_b], rider=rider)


def _ffn_in_swiglu(h2, w_blocked, name):
    s, k = h2.shape
    nblk, _, tn = w_blocked.shape
    ff = nblk // 2 * tn
    tm = 512

    def body(a_ref, wg_ref, wu_ref, g_ref, u_ref, act_ref):
        a = a_ref[...]
        gt = jnp.dot(a, wg_ref[...], preferred_element_type=F32)
        up = jnp.dot(a, wu_ref[...], preferred_element_type=F32)
        g_ref[...] = gt
        u_ref[...] = up
        act_ref[...] = (gt * jax.nn.sigmoid(gt) * up).astype(BF16)

    out = pl.BlockSpec((tm, tn), lambda i, j: (i, j))
    return pl.pallas_call(
        body, name=name, grid=(s // tm, nblk // 2),
        in_specs=[pl.BlockSpec((tm, k), lambda i, j: (i, 0)),
                  pl.BlockSpec((None, k, tn), lambda i, j: (j, 0, 0)),
                  pl.BlockSpec((None, k, tn), lambda i, j: (nblk // 2 + j, 0, 0))],
        out_specs=[out, out, out],
        out_shape=[jax.ShapeDtypeStruct((s, ff), F32), jax.ShapeDtypeStruct((s, ff), F32),
                   jax.ShapeDtypeStruct((s, ff), BF16)],
        compiler_params=_params(),
    )(h2, w_blocked, w_blocked)


def _swiglu_bwd(gate, up, d_act, name, rider=None):
    s, ff = gate.shape
    tm = 256

    def body(g_ref, u_ref, d_ref, o_ref):
        gt = g_ref[...]
        sg = jax.nn.sigmoid(gt)
        dv = d_ref[...]
        o_ref[:, 0:ff] = (dv * u_ref[...] * (sg * (1.0 + gt * (1.0 - sg)))).astype(BF16)
        o_ref[:, ff:2 * ff] = (dv * gt * sg).astype(BF16)

    row = pl.BlockSpec((tm, ff), lambda i: (i, 0))
    return _pallas(
        body, name=name, grid=(s // tm,), in_specs=[row, row, row],
        out_specs=pl.BlockSpec((tm, 2 * ff), lambda i: (i, 0)),
        out_shape=jax.ShapeDtypeStruct((s, 2 * ff), BF16), operands=[gate, up, d_act], rider=rider)


def _out_proj_rmsnorm(mixed, w_out, x, norm_w, name):
    s, k = mixed.shape
    d = w_out.shape[1]
    tm = 512

    def body(a_ref, w_ref, x_ref, nw_ref, x1_ref, h2_ref):
        x1 = x_ref[...] + jnp.dot(a_ref[...], w_ref[...], preferred_element_type=F32)
        x1_ref[...] = x1
        rstd = lax.rsqrt(jnp.mean(x1 * x1, axis=-1, keepdims=True) + EPS)
        h2_ref[...] = (x1 * rstd * nw_ref[...]).astype(BF16)

    row = pl.BlockSpec((tm, d), lambda i: (i, 0))
    return pl.pallas_call(
        body, name=name, grid=(s // tm,),
        in_specs=[pl.BlockSpec((tm, k), lambda i: (i, 0)), pl.BlockSpec((k, d), lambda i: (0, 0)), row,
                  pl.BlockSpec((1, d), lambda i: (0, 0))],
        out_specs=[row, row],
        out_shape=[jax.ShapeDtypeStruct((s, d), F32), jax.ShapeDtypeStruct((s, d), BF16)],
        compiler_params=_params(),
    )(mixed, w_out, x, norm_w)


def _ffn_out_loss(act, w_ffn_out, x1, target, name):
    s, k = act.shape
    d = w_ffn_out.shape[1]
    tm = 512

    def body(a_ref, w_ref, x1_ref, t_ref, dy_ref, dyb_ref, loss_ref, acc):
        y = x1_ref[...] + jnp.dot(a_ref[...], w_ref[...], preferred_element_type=F32)
        diff = y - t_ref[...]
        dy = diff * (1.0 / d)
        dy_ref[...] = dy
        dyb_ref[...] = dy.astype(BF16)
        part = jnp.sum((diff * diff).reshape(tm // 8, 8, d), axis=0)
        i = pl.program_id(0)

        @pl.when(i == 0)
        def _():
            acc[...] = part

        @pl.when(i > 0)
        def _():
            acc[...] += part

        @pl.when(i == pl.num_programs(0) - 1)
        def _():
            loss_ref[...] = (0.5 / d) * jnp.sum(jnp.sum(acc[...], axis=1, keepdims=True), axis=0, keepdims=True)

    row = pl.BlockSpec((tm, d), lambda i: (i, 0))
    return pl.pallas_call(
        body, name=name, grid=(s // tm,),
        in_specs=[pl.BlockSpec((tm, k), lambda i: (i, 0)), pl.BlockSpec((k, d), lambda i: (0, 0)), row, row],
        out_specs=[row, row, pl.BlockSpec((1, 1), lambda i: (0, 0))],
        out_shape=[jax.ShapeDtypeStruct((s, d), F32), jax.ShapeDtypeStruct((s, d), BF16),
                   jax.ShapeDtypeStruct((1, 1), F32)],
        scratch_shapes=[pltpu.VMEM((8, d), F32)], compiler_params=_params(),
    )(act, w_ffn_out, x1, target)


LATE_GATHER = ("w_o_attn", "w_pw_conv", "w_out", "w_ffn_in", "w_ffn_out")
EARLY_REDUCE = LATE_GATHER


def _blocks_by_half(g):
    if g.ndim == 2:
        g = g.reshape(N_CHIPS, g.shape[0] // N_CHIPS, g.shape[1])
    return g.reshape(N_CHIPS, 2, g.shape[1] // 2, g.shape[2])


def _forward_backward(x, pos_col, target, wts, late_bufs, pos_arr):
    wts = dict(wts)
    consts = _rope_consts()
    bd = consts[3]
    qw2 = jnp.tile(wts["q_norm_w"], (1, LANES // HEAD_DIM))
    kw2 = jnp.tile(wts["k_norm_w"], (1, LANES // HEAD_DIM))
    qkv_w = 3 * N_SLOT_HEADS * HEAD_DIM
    conv_col0 = 3 * qkv_w
    ch = wts["conv_w"].shape[1]
    gate_col0 = conv_col0 + 2 * ch

    h = _rmsnorm_fwd(x, wts["norm1_w"], "rms1_fwd")
    late = dict(zip(LATE_GATHER, late_bufs))
    quarter = late["w_ffn_in"].shape[2] // 4
    proj, (late["w_o_attn"], late["w_pw_conv"], late["w_out"], late["w_ffn_in"]) = _matmul(
        h, wts["w_in"], mode="nn", tm=512, tn=1920, tk=1024, out_dtype=F32, name="mm_proj", b_blocked=True,
        rider=_gather_ici_rider([late["w_o_attn"], late["w_pw_conv"], late["w_out"], late["w_ffn_in"]], [],
                                row_ranges=[None, None, None, (0, quarter)]))
    (qn, kn), (late["w_ffn_in"],) = _qk_fwd(
        proj, pos_col, qw2, kw2, consts, "qk_fwd",
        rider=_gather_ici_rider([late["w_ffn_in"]], [], row_ranges=[(quarter, 2 * quarter)]))
    (attn, lse, attn_b), (late["w_ffn_in"], late["w_ffn_out"]) = _attn_fwd(
        qn, kn, proj, "attn_fwd",
        rider=_gather_ici_rider([late["w_ffn_in"], late["w_ffn_out"]], [],
                                row_ranges=[(3 * quarter, quarter), None]))
    (u0, uc), late_bufs = _conv_fwd(proj, wts["conv_w"], wts["conv_b"], conv_col0, "conv_fwd",
                                    rider=_gather_forward_rider([late[n] for n in LATE_GATHER]))
    for n, buf in zip(LATE_GATHER, late_bufs):
        full = buf.reshape(N_CHIPS, -1, buf.shape[3])
        wts[n] = full.reshape(-1, full.shape[2]) if n in ROW_SHARDED else full
    y_a = _matmul(attn_b, wts["w_o_attn"], mode="nn", tm=1024, tn=256, tk=512, out_dtype=F32, name="mm_ya",
                  b_blocked=True)
    u3 = _ln_silu_fwd(uc, wts["conv_ln_w"], wts["conv_ln_b"], "ln_fwd")
    y_b = _matmul(u3, wts["w_pw_conv"], mode="nn", tm=1024, tn=256, tk=512, out_dtype=F32, name="mm_yb",
                  b_blocked=True)
    mixed = _gate_fwd(proj, wts["b_gate"], y_a, y_b, gate_col0, "gate_fwd")
    x1, h2 = _out_proj_rmsnorm(mixed, wts["w_out"], x, wts["norm2_w"], "mm_x1_rms2")
    gate, up, act = _ffn_in_swiglu(h2, wts["w_ffn_in"], "mm_gu_swiglu")
    dy, dy_b16, loss = _ffn_out_loss(act, wts["w_ffn_out"], x1, target, "mm_x2_loss")

    g = {}
    by_chip = {}

    def pair_add(n, blocks, received):
        return _add_own_half(blocks, received, pos_arr, f"grads_pair_add_{n}")

    d_act = _matmul(dy_b16, wts["w_ffn_out"], mode="nt", tm=512, tn=1408, tk=1024, out_dtype=F32, name="mm_dact")
    g_ffn_out = _blocks_by_half(
        _matmul(act, dy_b16, mode="tn", tm=1408, tn=1024, tk=2048, out_dtype=F32, name="mm_dwffnout"))
    dgu, (received,) = _swiglu_bwd(gate, up, d_act, "swiglu_bwd",
                                   rider=_pair_exchange_rider([g_ffn_out], halved=True))
    to_send, own = pair_add("w_ffn_out", g_ffn_out, received)
    dh2, (by_chip["w_ffn_out"],) = _matmul(
        dgu, wts["w_ffn_in"], mode="nt", tm=1024, tn=1024, tk=1408, out_dtype=F32, name="mm_dh2", b_blocked=True,
        rider=_chip_exchange_rider([to_send], [own]))
    g_ffn_in = _blocks_by_half(_matmul(h2, dgu, mode="tn", tm=512, tn=1408, tk=2048, out_dtype=F32,
                                       name="mm_dwffnin", out_blocked=N_CHIPS))
    dx1, dx1_b16, g["norm2_w"] = _rmsnorm_bwd(dh2, x1, wts["norm2_w"], dy, "rms2_bwd")
    d_mixed = _matmul(dx1_b16, wts["w_out"], mode="nt", tm=512, tn=1024, tk=1024, out_dtype=F32, name="mm_dmixed")
    g["w_out"] = _matmul(mixed, dx1_b16, mode="tn", tm=512, tn=1024, tk=2048, out_dtype=F32, name="mm_dwout")
    (dgl, dy_a, dy_b, g["b_gate"]), (received,) = _gate_bwd(
        d_mixed, proj, wts["b_gate"], y_a, y_b, gate_col0, "gate_bwd",
        rider=_pair_exchange_rider([g_ffn_in], halved=True))
    ffn_in_to_send, ffn_in_own = pair_add("w_ffn_in", g_ffn_in, received)
    dattn = _matmul(dy_a, wts["w_o_attn"], mode="nt", tm=1024, tn=512, tk=256, out_dtype=F32, name="mm_dattn",
                    b_blocked=True)
    g["w_o_attn"] = _matmul(attn_b, dy_a, mode="tn", tm=512, tn=256, tk=2048, out_dtype=F32, name="mm_dwo",
                            out_blocked=N_CHIPS)
    du3 = _matmul(dy_b, wts["w_pw_conv"], mode="nt", tm=1024, tn=512, tk=256, out_dtype=F32, name="mm_du3",
                  b_blocked=True)
    g["w_pw_conv"] = _matmul(u3, dy_b, mode="tn", tm=512, tn=256, tk=2048, out_dtype=F32, name="mm_dwpw",
                             out_blocked=N_CHIPS)
    duc, g["conv_ln_w"], g["conv_ln_b"] = _ln_silu_bwd(du3, uc, wts["conv_ln_w"], wts["conv_ln_b"], "ln_bwd")

    small3 = ("w_out", "w_o_attn", "w_pw_conv")
    g_small3 = [_blocks_by_half(g.pop(n)) for n in small3]
    (da, db, g["conv_w"], g["conv_b"]), received = _conv_bwd(
        duc, u0, proj, wts["conv_w"], conv_col0, "conv_bwd", rider=_pair_exchange_rider(g_small3, halved=True))
    sums3 = [pair_add(n, gb, rv) for n, gb, rv in zip(small3, g_small3, received)]
    (dqn, dkn, dv), (by_chip["w_ffn_in"],) = _attn_bwd(
        qn, kn, proj, dattn, attn, lse, bd, "attn_bwd",
        rider=_chip_exchange_rider([ffn_in_to_send], [ffn_in_own]))
    (dproj, dqw, dkw), exchanged3 = _qk_bwd(
        dqn, dkn, dv, da, db, dgl, proj, pos_col, qw2, kw2, consts, "qk_bwd",
        rider=_chip_exchange_rider([s[0] for s in sums3], [s[1] for s in sums3]))
    by_chip.update(zip(small3, exchanged3))
    halves = [_sum_chips(by_chip[n], pos_arr, f"grads_chip_sum_{n}") for n in EARLY_REDUCE]
    g["q_norm_w"] = dqw[:, :HEAD_DIM]
    g["k_norm_w"] = dkw[:, :HEAD_DIM]

    c = pos_arr[0]
    rh = h.shape[1] // 2
    h_sibling = lax.dynamic_slice_in_dim(h, (1 - c) * rh, rh, axis=1)
    h_own = lax.dynamic_slice_in_dim(h, c * rh, rh, axis=1)
    g_sibling, shards = _matmul(h_sibling, dproj, mode="tn", tm=rh, tn=1920, tk=2048, out_dtype=F32,
                                name="mm_dwin_sibling", out_blocked=N_CHIPS, rider=_pair_gather_rider(halves))
    reduced = dict(zip(EARLY_REDUCE, shards))
    g_own, from_sibling = _matmul(h_own, dproj, mode="tn", tm=rh, tn=1920, tk=2048, out_dtype=F32,
                                  name="mm_dwin_own", out_blocked=N_CHIPS,
                                  rider=_pair_exchange_rider([g_sibling], halved=False))
    to_send, own = _add_own_half(g_own, from_sibling[0], pos_arr, "grads_pair_add_w_in")
    first_rows = 288
    dh, by_chip_w_in = _matmul(dproj, wts["w_in"], mode="nt", tm=1024, tn=1024, tk=1920, out_dtype=F32, name="mm_dh",
                               b_blocked=True,
                               rider=_chip_exchange_rider([to_send], [own], row_range=(0, first_rows)))
    (grad_x, _, g["norm1_w"]), by_chip_w_in = _rmsnorm_bwd(
        dh, x, wts["norm1_w"], dx1, "rms1_bwd",
        rider=_chip_exchange_rider([to_send], by_chip_w_in, row_range=(first_rows, rh - first_rows)))
    return loss, grad_x, g, reduced, by_chip_w_in[0]


def _mesh_pos():
    return lax.axis_index("x"), lax.axis_index("y"), lax.axis_index("c")


def _other_chips(x, y):
    return [(1 - x, y), (x, 1 - y), (1 - x, 1 - y)]


def _cast_into_slot(shard, chip_arr, dtype, name):
    r, c = shard.shape
    tr = r // 2 if r % 32 == 0 else r

    def body(chip_ref, s_ref, o_ref):
        del chip_ref
        o_ref[...] = s_ref[...].astype(dtype)

    return pl.pallas_call(
        body, name=name,
        grid_spec=pltpu.PrefetchScalarGridSpec(
            num_scalar_prefetch=1, grid=(r // tr,),
            in_specs=[pl.BlockSpec((tr, c), lambda i, chip_ref: (i, 0))],
            out_specs=pl.BlockSpec((None, tr, c), lambda i, chip_ref: (chip_ref[0], i, 0))),
        out_shape=jax.ShapeDtypeStruct((N_CHIPS, r, c), dtype), compiler_params=_params(),
    )(chip_arr, shard)


def _allgather_inplace(big, small, name):
    nb, ns = len(big), len(small)
    n = nb + ns

    def body(*refs):
        bufs = refs[n:2 * n]
        send_sems, recv_sems, fsend_sems, frecv_sems = refs[2 * n:]
        x, y, c = _mesh_pos()
        me = 2 * x + y
        chips = _other_chips(x, y)

        def part(a, slot, half):
            return bufs[a].at[slot, half] if a < nb else bufs[a].at[slot]

        sends = []
        for a in range(n):
            for k, (px, py) in enumerate(chips):
                cp = pltpu.make_async_remote_copy(
                    src_ref=part(a, me, c), dst_ref=part(a, me, c), send_sem=send_sems.at[a, k],
                    recv_sem=recv_sems.at[a, k], device_id=(px, py, c), device_id_type=MESH)
                cp.start()
                sends.append(cp)
        for a in range(n):
            for k, (px, py) in enumerate(chips):
                slot = 2 * px + py
                pltpu.make_async_remote_copy(
                    src_ref=part(a, slot, c), dst_ref=part(a, slot, c), send_sem=send_sems.at[a, k],
                    recv_sem=recv_sems.at[a, k], device_id=(px, py, c), device_id_type=MESH).wait_recv()
                if a < nb:
                    fwd = pltpu.make_async_remote_copy(
                        src_ref=part(a, slot, c), dst_ref=part(a, slot, c), send_sem=fsend_sems.at[a, k],
                        recv_sem=frecv_sems.at[a, k], device_id=(x, y, 1 - c), device_id_type=MESH)
                    fwd.start()
                    sends.append(fwd)
        for a in range(nb):
            for k, (px, py) in enumerate(chips):
                slot = 2 * px + py
                pltpu.make_async_remote_copy(
                    src_ref=part(a, slot, 1 - c), dst_ref=part(a, slot, 1 - c), send_sem=fsend_sems.at[a, k],
                    recv_sem=frecv_sems.at[a, k], device_id=(x, y, 1 - c), device_id_type=MESH).wait_recv()
        for cp in sends:
            cp.wait_send()

    ops = list(big) + list(small)
    return pl.pallas_call(
        body, name=name, in_specs=[ANY] * n, out_specs=[ANY] * n,
        out_shape=[jax.ShapeDtypeStruct(o.shape, o.dtype) for o in ops],
        input_output_aliases={i: i for i in range(n)},
        scratch_shapes=[pltpu.SemaphoreType.DMA((n, 3)), pltpu.SemaphoreType.DMA((n, 3)),
                        pltpu.SemaphoreType.DMA((nb, 3)), pltpu.SemaphoreType.DMA((nb, 3))],
    )(*ops)


def _comm_call(rider, name):
    def body():
        pass

    return _pallas(body, name=name, grid=(1,), in_specs=[], out_specs=[], out_shape=[], operands=[],
                   rider=rider)[1]


def _gather_ici_rider(big, small, row_ranges=None):
    nb = len(big)
    n = nb + len(small)
    row_ranges = row_ranges or [None] * nb

    def copies(bufs, sems):
        x, y, c = _mesh_pos()
        me = 2 * x + y

        def part(a, slot):
            if a >= nb:
                return bufs[a].at[slot]
            if row_ranges[a] is None:
                return bufs[a].at[slot, c]
            return bufs[a].at[slot, c, pl.ds(*row_ranges[a])]
        out = []
        for a in range(n):
            for k, (px, py) in enumerate(_other_chips(x, y)):
                send = functools.partial(
                    pltpu.make_async_remote_copy,
                    src_ref=part(a, me), dst_ref=part(a, me), send_sem=sems[0].at[a, k],
                    recv_sem=sems[1].at[a, k], device_id=(px, py, c), device_id_type=MESH)
                recv = functools.partial(
                    pltpu.make_async_remote_copy,
                    src_ref=part(a, 2 * px + py), dst_ref=part(a, 2 * px + py), send_sem=sems[0].at[a, k],
                    recv_sem=sems[1].at[a, k], device_id=(px, py, c), device_id_type=MESH)
                out.append((send, recv))
        return out

    def start(r_in, r_out, sems):
        for send, _ in copies(r_out, sems):
            send().start()

    def wait(r_in, r_out, sems):
        cps = copies(r_out, sems)
        for _, recv in cps:
            recv().wait_recv()
        for send, _ in cps:
            send().wait_send()

    ops = list(big) + list(small)
    return _Rider(ops, [jax.ShapeDtypeStruct(o.shape, o.dtype) for o in ops], {i: i for i in range(n)},
                  [pltpu.SemaphoreType.DMA((n, 3)), pltpu.SemaphoreType.DMA((n, 3))], start, wait)


def _gather_forward_rider(big):
    n = len(big)

    def copies(bufs, sems):
        x, y, c = _mesh_pos()
        out = []
        for a in range(n):
            for k, (px, py) in enumerate(_other_chips(x, y)):
                slot = 2 * px + py
                send = functools.partial(
                    pltpu.make_async_remote_copy,
                    src_ref=bufs[a].at[slot, c], dst_ref=bufs[a].at[slot, c], send_sem=sems[0].at[a, k],
                    recv_sem=sems[1].at[a, k], device_id=(x, y, 1 - c), device_id_type=MESH)
                recv = functools.partial(
                    pltpu.make_async_remote_copy,
                    src_ref=bufs[a].at[slot, 1 - c], dst_ref=bufs[a].at[slot, 1 - c], send_sem=sems[0].at[a, k],
                    recv_sem=sems[1].at[a, k], device_id=(x, y, 1 - c), device_id_type=MESH)
                out.append((send, recv))
        return out

    def start(r_in, r_out, sems):
        for send, _ in copies(r_out, sems):
            send().start()

    def wait(r_in, r_out, sems):
        cps = copies(r_out, sems)
        for _, recv in cps:
            recv().wait_recv()
        for send, _ in cps:
            send().wait_send()

    return _Rider(big, [jax.ShapeDtypeStruct(o.shape, o.dtype) for o in big], {i: i for i in range(n)},
                  [pltpu.SemaphoreType.DMA((n, 3)), pltpu.SemaphoreType.DMA((n, 3))], start, wait)


def _pair_exchange_rider(gs, halved):
    n = len(gs)

    def copies(r_in, r_out, sems):
        x, y, c = _mesh_pos()
        return [pltpu.make_async_remote_copy(
            src_ref=r_in[a].at[:, 1 - c] if halved else r_in[a], dst_ref=r_out[a], send_sem=sems[0].at[a],
            recv_sem=sems[1].at[a], device_id=(x, y, 1 - c), device_id_type=MESH) for a in range(n)]

    def start(r_in, r_out, sems):
        for cp in copies(r_in, r_out, sems):
            cp.start()

    def wait(r_in, r_out, sems):
        for cp in copies(r_in, r_out, sems):
            cp.wait()

    return _Rider(gs, [jax.ShapeDtypeStruct((g.shape[0],) + g.shape[-2:], g.dtype) for g in gs], {},
                  [pltpu.SemaphoreType.DMA((n,)), pltpu.SemaphoreType.DMA((n,))], start, wait)


def _chip_exchange_rider(to_send, by_chip, row_range=None):
    n = len(to_send)

    def copies(r_in, r_out, sems):
        x, y, c = _mesh_pos()
        me = 2 * x + y
        rows = (lambda ref: ref) if row_range is None else (lambda ref: ref.at[pl.ds(*row_range)])
        out = []
        for a in range(n):
            for k, (px, py) in enumerate(_other_chips(x, y)):
                send = functools.partial(
                    pltpu.make_async_remote_copy,
                    src_ref=rows(r_in[a].at[2 * px + py]), dst_ref=rows(r_out[a].at[me]),
                    send_sem=sems[0].at[a, k], recv_sem=sems[1].at[a, k], device_id=(px, py, c),
                    device_id_type=MESH)
                recv = functools.partial(
                    pltpu.make_async_remote_copy,
                    src_ref=rows(r_in[a].at[me]), dst_ref=rows(r_out[a].at[2 * px + py]),
                    send_sem=sems[0].at[a, k], recv_sem=sems[1].at[a, k], device_id=(px, py, c),
                    device_id_type=MESH)
                out.append((send, recv))
        return out

    def start(r_in, r_out, sems):
        for send, _ in copies(r_in, r_out, sems):
            send().start()

    def wait(r_in, r_out, sems):
        cps = copies(r_in, r_out, sems)
        for _, recv in cps:
            recv().wait_recv()
        for send, _ in cps:
            send().wait_send()

    return _Rider(list(to_send) + list(by_chip), [jax.ShapeDtypeStruct(b.shape, b.dtype) for b in by_chip],
                  {n + i: i for i in range(n)},
                  [pltpu.SemaphoreType.DMA((n, 3)), pltpu.SemaphoreType.DMA((n, 3))], start, wait)


def _pair_gather_rider(bufs):
    n = len(bufs)

    def copies(r_out, sems):
        x, y, c = _mesh_pos()
        out = []
        for a in range(n):
            send = functools.partial(
                    pltpu.make_async_remote_copy,
                src_ref=r_out[a].at[c], dst_ref=r_out[a].at[c], send_sem=sems[0].at[a],
                recv_sem=sems[1].at[a], device_id=(x, y, 1 - c), device_id_type=MESH)
            recv = functools.partial(
                    pltpu.make_async_remote_copy,
                src_ref=r_out[a].at[1 - c], dst_ref=r_out[a].at[1 - c], send_sem=sems[0].at[a],
                recv_sem=sems[1].at[a], device_id=(x, y, 1 - c), device_id_type=MESH)
            out.append((send, recv))
        return out

    def start(r_in, r_out, sems):
        for send, _ in copies(r_out, sems):
            send().start()

    def wait(r_in, r_out, sems):
        cps = copies(r_out, sems)
        for _, recv in cps:
            recv().wait_recv()
        for send, _ in cps:
            send().wait_send()

    return _Rider(bufs, [jax.ShapeDtypeStruct(b.shape, b.dtype) for b in bufs], {i: i for i in range(n)},
                  [pltpu.SemaphoreType.DMA((n,)), pltpu.SemaphoreType.DMA((n,))], start, wait)


def _add_own_half(g, recv, pos_arr, name):
    nb, rh, cols = g.shape[0], g.shape[-2], g.shape[-1]

    def body(pos_ref, g_ref, r_ref, send_ref, own_ref):
        s = (g_ref[...] + r_ref[...]).astype(BF16)
        send_ref[...] = s

        @pl.when(pl.program_id(0) == pos_ref[1])
        def _():
            own_ref[...] = s

    blk = pl.BlockSpec((None, rh, cols), lambda j, pos_ref: (j, 0, 0))
    g_spec = blk if g.ndim == 3 else pl.BlockSpec((None, None, rh, cols),
                                                   lambda j, pos_ref: (j, pos_ref[0], 0, 0))
    shape = jax.ShapeDtypeStruct((nb, rh, cols), BF16)
    return pl.pallas_call(
        body, name=name,
        grid_spec=pltpu.PrefetchScalarGridSpec(
            num_scalar_prefetch=1, grid=(nb,), in_specs=[g_spec, blk],
            out_specs=[blk, pl.BlockSpec((None, rh, cols), lambda j, pos_ref: (pos_ref[1], 0, 0))]),
        out_shape=[shape, shape], compiler_params=_params(),
    )(pos_arr, g, recv)


def _sum_chips(gath, pos_arr, name):
    nb, rh, cols = gath.shape

    def body(pos_ref, a_ref, b_ref, c_ref, d_ref, o_ref):
        del pos_ref
        o_ref[...] = ((a_ref[...].astype(F32) + b_ref[...].astype(F32)) + c_ref[...].astype(F32)) \
            + d_ref[...].astype(F32)

    tr = rh // 2 if (rh // 2) % 16 == 0 else rh
    specs = [pl.BlockSpec((None, tr, cols), functools.partial(lambda i, pos_ref, j: (j, i, 0), j=j))
             for j in range(nb)]
    return pl.pallas_call(
        body, name=name,
        grid_spec=pltpu.PrefetchScalarGridSpec(
            num_scalar_prefetch=1, grid=(rh // tr,), in_specs=specs,
            out_specs=pl.BlockSpec((None, tr, cols), lambda i, pos_ref: (pos_ref[0], i, 0))),
        out_shape=jax.ShapeDtypeStruct((2, rh, cols), F32), compiler_params=_params(),
    )(pos_arr, gath, gath, gath, gath)


def _small_allreduce(v, name, rider=None):
    n = v.shape[0]
    n_dev = 8

    def body(v_ref, o_ref, buf, send_sems, recv_sems):
        x, y, c = _mesh_pos()
        me = 4 * x + 2 * y + c
        buf[me] = v_ref[...]
        sends = []
        peers = []
        for r in range(1, n_dev):
            px = 1 - x if r & 4 else x
            py = 1 - y if r & 2 else y
            pc = 1 - c if r & 1 else c
            peers.append((px, py, pc))
            cp = pltpu.make_async_remote_copy(
                src_ref=v_ref, dst_ref=buf.at[me], send_sem=send_sems.at[r - 1],
                recv_sem=recv_sems.at[r - 1], device_id=(px, py, pc), device_id_type=MESH)
            cp.start()
            sends.append(cp)
        for r, (px, py, pc) in enumerate(peers):
            pltpu.make_async_remote_copy(
                src_ref=v_ref, dst_ref=buf.at[4 * px + 2 * py + pc], send_sem=send_sems.at[r],
                recv_sem=recv_sems.at[r], device_id=(px, py, pc), device_id_type=MESH).wait_recv()
        for cp in sends:
            cp.wait_send()
        acc = buf[0]
        for i in range(1, n_dev):
            acc = acc + buf[i]
        o_ref[...] = acc

    whole = pl.BlockSpec(v.shape, lambda i: (0, 0))
    return _pallas(
        body, name=name, grid=(1,), in_specs=[whole], out_specs=whole,
        out_shape=jax.ShapeDtypeStruct(v.shape, v.dtype), operands=[v],
        scratch_shapes=[pltpu.VMEM((n_dev, n, LANES), F32), pltpu.SemaphoreType.DMA((n_dev - 1,)),
                        pltpu.SemaphoreType.DMA((n_dev - 1,))],
        rider=rider)


def _adamw_math(w, g, m, v):
    m = ADAM_B1 * m + (1.0 - ADAM_B1) * g
    v = ADAM_B2 * v + (1.0 - ADAM_B2) * (g * g)
    m_hat = m / (1.0 - ADAM_B1 ** ADAM_STEP)
    v_hat = v / (1.0 - ADAM_B2 ** ADAM_STEP)
    delta = -ADAM_LR * (m_hat / (jnp.sqrt(v_hat) + ADAM_EPS) + ADAM_WD * w)
    return delta, m, v


def _adamw(w, g, m, v, name):
    r, c = w.shape
    tr = 128 if r % 128 == 0 else 64
    assert r % tr == 0

    def body(w_ref, g_ref, m_ref, v_ref, go_ref, d_ref, mo_ref, vo_ref):
        gv = g_ref[...]
        d, mn, vn = _adamw_math(w_ref[...], gv, m_ref[...], v_ref[...])
        go_ref[...] = gv
        d_ref[...] = d
        mo_ref[...] = mn
        vo_ref[...] = vn

    blk = pl.BlockSpec((tr, c), lambda i: (i, 0))
    return _pallas(body, name=name, grid=(r // tr,), in_specs=[blk] * 4, out_specs=[blk] * 4,
                   out_shape=[jax.ShapeDtypeStruct((r, c), F32)] * 4, operands=[w, g, m, v])


def _adamw_small(ws, gs, ms, vs, name):
    n = len(ws)

    def body(*refs):
        w_r, g_r, m_r, v_r = refs[:n], refs[n:2 * n], refs[2 * n:3 * n], refs[3 * n:4 * n]
        d_o, m_o, v_o = refs[4 * n:5 * n], refs[5 * n:6 * n], refs[6 * n:7 * n]
        for i in range(n):
            d, mn, vn = _adamw_math(w_r[i][...], g_r[i][...], m_r[i][...], v_r[i][...])
            d_o[i][...] = d
            m_o[i][...] = mn
            v_o[i][...] = vn

    specs = [pl.BlockSpec(w.shape, lambda i: (0, 0)) for w in ws]
    shapes = [jax.ShapeDtypeStruct(w.shape, F32) for w in ws]
    outs = pl.pallas_call(
        body, name=name, grid=(1,), in_specs=specs * 4, out_specs=specs * 3, out_shape=shapes * 3,
        compiler_params=_params(),
    )(*ws, *gs, *ms, *vs)
    return outs[:n], outs[n:2 * n], outs[2 * n:]


BIG = ("w_in", "w_o_attn", "w_pw_conv", "w_out", "w_ffn_in", "w_ffn_out")
ROW_SHARDED = ("w_out", "w_ffn_out")
SMALL = ("norm1_w", "b_gate", "q_norm_w", "k_norm_w", "conv_w", "conv_b", "conv_ln_w", "conv_ln_b", "norm2_w")
ORDER = ("norm1_w", "w_in", "b_gate", "q_norm_w", "k_norm_w", "w_o_attn", "conv_w", "conv_b", "conv_ln_w",
         "conv_ln_b", "w_pw_conv", "w_out", "norm2_w", "w_ffn_in", "w_ffn_out")
PACK_TILE = 8 * LANES


def _pack_small(parts):
    rows = []
    for p in parts:
        flat = p.reshape(-1)
        pad = (-flat.shape[0]) % PACK_TILE
        rows.append(jnp.pad(flat, (0, pad)).reshape(-1, LANES))
    return jnp.concatenate(rows, axis=0)


def _unpack_small(packed, shapes):
    out, row = [], 0
    for shp in shapes:
        size = int(np.prod(shp))
        nrow = -(-size // PACK_TILE) * (PACK_TILE // LANES)
        out.append(packed[row:row + nrow].reshape(-1)[:size].reshape(shp))
        row += nrow
    return out


def kernel(x, positions, norm1_w, w_in, b_gate, q_norm_w, k_norm_w, w_o_attn, conv_w, conv_b, conv_ln_w, conv_ln_b, w_pw_conv, w_out, norm2_w, w_ffn_in, w_ffn_out, loss_target, m_norm1_w, m_w_in, m_b_gate, m_q_norm_w, m_k_norm_w, m_w_o_attn, m_conv_w, m_conv_b, m_conv_ln_w, m_conv_ln_b, m_w_pw_conv, m_w_out, m_norm2_w, m_w_ffn_in, m_w_ffn_out, v_norm1_w, v_w_in, v_b_gate, v_q_norm_w, v_k_norm_w, v_w_o_attn, v_conv_w, v_conv_b, v_conv_ln_w, v_conv_ln_b, v_w_pw_conv, v_w_out, v_norm2_w, v_w_ffn_in, v_w_ffn_out):
    w = dict(norm1_w=norm1_w, w_in=w_in, b_gate=b_gate, q_norm_w=q_norm_w, k_norm_w=k_norm_w, w_o_attn=w_o_attn,
             conv_w=conv_w, conv_b=conv_b, conv_ln_w=conv_ln_w, conv_ln_b=conv_ln_b, w_pw_conv=w_pw_conv,
             w_out=w_out, norm2_w=norm2_w, w_ffn_in=w_ffn_in, w_ffn_out=w_ffn_out)
    m = dict(norm1_w=m_norm1_w, w_in=m_w_in, b_gate=m_b_gate, q_norm_w=m_q_norm_w, k_norm_w=m_k_norm_w,
             w_o_attn=m_w_o_attn, conv_w=m_conv_w, conv_b=m_conv_b, conv_ln_w=m_conv_ln_w,
             conv_ln_b=m_conv_ln_b, w_pw_conv=m_w_pw_conv, w_out=m_w_out, norm2_w=m_norm2_w,
             w_ffn_in=m_w_ffn_in, w_ffn_out=m_w_ffn_out)
    v = dict(norm1_w=v_norm1_w, w_in=v_w_in, b_gate=v_b_gate, q_norm_w=v_q_norm_w, k_norm_w=v_k_norm_w,
             w_o_attn=v_w_o_attn, conv_w=v_conv_w, conv_b=v_conv_b, conv_ln_w=v_conv_ln_w,
             conv_ln_b=v_conv_ln_b, w_pw_conv=v_w_pw_conv, w_out=v_w_out, norm2_w=v_norm2_w,
             w_ffn_in=v_w_ffn_in, w_ffn_out=v_w_ffn_out)
    cx, cy, cc = _mesh_pos()
    chip = 2 * cx + cy

    chip_arr = chip.reshape(1).astype(jnp.int32)
    pos_arr = jnp.stack([cc, chip]).astype(jnp.int32)
    bufs = {}
    for n in BIG:
        buf = _cast_into_slot(w[n][0], chip_arr, BF16, f"cast_{n}")
        bufs[n] = buf.reshape(N_CHIPS, 2, buf.shape[1] // 2, buf.shape[2])
    small_bufs = [_cast_into_slot(w[n][0], chip_arr, F32, f"slot_{n}") for n in ("conv_w", "b_gate")]
    w_in_buf, conv_w_buf, b_gate_buf = _allgather_inplace([bufs["w_in"]], small_bufs, "allgather_w_in")
    wts = dict(w_in=w_in_buf.reshape(N_CHIPS, -1, w_in_buf.shape[3]),
               conv_w=conv_w_buf.transpose(1, 0, 2).reshape(CONV_WIDTH, -1),
               b_gate=b_gate_buf.transpose(1, 0, 2).reshape(2, 1, -1),
               norm1_w=norm1_w, q_norm_w=q_norm_w, k_norm_w=k_norm_w, conv_b=conv_b, conv_ln_w=conv_ln_w,
               conv_ln_b=conv_ln_b, norm2_w=norm2_w)

    loss, grad_x, g, reduced, by_chip_w_in = _forward_backward(
        x[0], positions.reshape(-1, 1), loss_target[0], wts, [bufs[n] for n in LATE_GATHER], pos_arr)
    grads = {n: b.reshape(-1, b.shape[2]) for n, b in reduced.items()}
    half_w_in = _sum_chips(by_chip_w_in, pos_arr, "grads_chip_sum_w_in")

    small_parts = [loss] + [g[n] for n in SMALL]
    small_shapes = [p.shape for p in small_parts]
    summed, (shard_w_in,) = _small_allreduce(_pack_small(small_parts), "small_allreduce",
                                             rider=_pair_gather_rider([half_w_in]))
    grads["w_in"] = shard_w_in.reshape(-1, shard_w_in.shape[2])
    reduced = _unpack_small(summed, small_shapes)
    loss_total = reduced[0].reshape(())
    for n, r in zip(SMALL, reduced[1:]):
        grads[n] = r
    ch_shard = conv_w.shape[2]
    grads["conv_w"] = lax.dynamic_slice_in_dim(grads["conv_w"], chip * ch_shard, ch_shard, axis=1)
    d_shard = b_gate.shape[2]
    grads["b_gate"] = lax.dynamic_slice_in_dim(grads["b_gate"], chip * d_shard, d_shard, axis=1)

    delta, new_m, new_v = {}, {}, {}
    for n in BIG:
        grads[n], delta[n], new_m[n], new_v[n] = _adamw(w[n][0], grads[n], m[n][0], v[n][0], f"adamw_{n}")
    flat2 = lambda a: a.reshape(-1, a.shape[-1])
    d_s, m_s, v_s = _adamw_small([flat2(w[n]) for n in SMALL], [flat2(grads[n]) for n in SMALL],
                                 [flat2(m[n]) for n in SMALL], [flat2(v[n]) for n in SMALL], "adamw_small")
    for i, n in enumerate(SMALL):
        delta[n], new_m[n], new_v[n] = d_s[i], m_s[i], v_s[i]

    shaped = lambda d, n: d[n].reshape(w[n].shape)
    return (loss_total, grad_x[None], *[shaped(grads, n) for n in ORDER], *[shaped(delta, n) for n in ORDER],
            *[shaped(new_m, n) for n in ORDER], *[shaped(new_v, n) for n in ORDER])
```

```python
import functools

import numpy as np
import jax
import jax.numpy as jnp
from jax import lax
from jax.experimental import pallas as pl
from jax.experimental.pallas import tpu as pltpu

F32 = jnp.float32
BF16 = jnp.bfloat16
MESH = pl.DeviceIdType.MESH
ANY = pl.BlockSpec(memory_space=pl.ANY)

HEAD_DIM = 64
N_SLOT_HEADS = 8
DILATIONS = (1, 4, 16)
HALF_SPAN = 64
ROPE_THETA = 500000.0
ROT_DIM = 16
CONV_WIDTH = 31
EPS = 1e-6
NEG_INF = -1e30
ADAM_LR, ADAM_B1, ADAM_B2, ADAM_EPS, ADAM_WD, ADAM_STEP = 0.001, 0.9, 0.999, 1e-08, 0.01, 10

LANES = 128
QBLK = 128
KWIN = QBLK + 2 * HALF_SPAN
VMEM_LIMIT = 60 * 1024 * 1024
N_CHIPS = 4


def _params(**kw):
    return pltpu.CompilerParams(vmem_limit_bytes=VMEM_LIMIT, **kw)


class _Rider:
    def __init__(self, operands, out_shapes, aliases, scratch, start, wait):
        self.operands, self.out_shapes, self.aliases = list(operands), list(out_shapes), dict(aliases)
        self.scratch, self.start, self.wait = list(scratch), start, wait


def _pallas(body, *, name, grid, in_specs, out_specs, out_shape, operands, scratch_shapes=(), aliases=None,
            rider=None):
    single = not isinstance(out_specs, (list, tuple))
    out_specs_l = [out_specs] if single else list(out_specs)
    out_shape_l = [out_shape] if single else list(out_shape)
    aliases = dict(aliases or {})
    if rider is None:
        res = pl.pallas_call(
            body, name=name, grid=grid, in_specs=list(in_specs), out_specs=out_specs_l, out_shape=out_shape_l,
            scratch_shapes=list(scratch_shapes), input_output_aliases=aliases, compiler_params=_params(),
        )(*operands)
        return res[0] if single else res
    n_in, n_rin = len(in_specs), len(rider.operands)
    n_out, n_rout = len(out_specs_l), len(rider.out_shapes)
    n_sc = len(scratch_shapes)

    def wrapped(*refs):
        main_in, r_in = refs[:n_in], refs[n_in:n_in + n_rin]
        o0 = n_in + n_rin
        main_out, r_out = refs[o0:o0 + n_out], refs[o0 + n_out:o0 + n_out + n_rout]
        s0 = o0 + n_out + n_rout
        main_sc, r_sc = refs[s0:s0 + n_sc], refs[s0 + n_sc:]
        ids = [pl.program_id(d) for d in range(len(grid))]
        first = functools.reduce(jnp.logical_and, [i == 0 for i in ids])
        last = functools.reduce(jnp.logical_and, [i == n - 1 for i, n in zip(ids, grid)])

        @pl.when(first)
        def _():
            rider.start(r_in, r_out, r_sc)

        body(*main_in, *main_out, *main_sc)

        @pl.when(last)
        def _():
            rider.wait(r_in, r_out, r_sc)

    for src, dst in rider.aliases.items():
        aliases[n_in + src] = n_out + dst
    res = pl.pallas_call(
        wrapped, name=name, grid=grid, in_specs=list(in_specs) + [ANY] * n_rin,
        out_specs=out_specs_l + [ANY] * n_rout, out_shape=out_shape_l + rider.out_shapes,
        scratch_shapes=list(scratch_shapes) + rider.scratch, input_output_aliases=aliases,
        compiler_params=_params(),
    )(*operands, *rider.operands)
    main = res[:n_out]
    return (main[0] if single else main), res[n_out:]


def _matmul(a, b, *, mode, tm, tn, tk, out_dtype, name, b_blocked=False,
            out_blocked=None, rider=None):
    a_shape = a.shape
    if mode == "nn":
        m_dim, k_dim = a_shape
        n_dim = b.shape[0] * b.shape[2] if b_blocked else b.shape[1]
        rows, cols, red = m_dim, n_dim, k_dim
    elif mode == "nt":
        m_dim, n_dim = a_shape
        k_dim = b.shape[1] if b_blocked else b.shape[0]
        rows, cols, red = m_dim, k_dim, n_dim
    else:
        m_dim, k_dim = a_shape
        n_dim = b.shape[1]
        rows, cols, red = k_dim, n_dim, m_dim
    assert rows % tm == 0 and cols % tn == 0 and red % tk == 0, (name, rows, cols, red)
    ni, nj, nk = rows // tm, cols // tn, red // tk

    if mode == "nn":
        a_spec = pl.BlockSpec((tm, tk), lambda i, j, k: (i, k))
        if b_blocked:
            per = b.shape[2] // tn
            b_spec = pl.BlockSpec((None, tk, tn), lambda i, j, k: (j // per, k, j % per))
        else:
            b_spec = pl.BlockSpec((tk, tn), lambda i, j, k: (k, j))
        dims = (((1,), (0,)), ((), ()))
    elif mode == "nt":
        a_spec = pl.BlockSpec((tm, tk), lambda i, j, k: (i, k))
        if b_blocked:
            per = b.shape[2] // tk
            b_spec = pl.BlockSpec((None, tn, tk), lambda i, j, k: (k // per, j, k % per))
        else:
            b_spec = pl.BlockSpec((tn, tk), lambda i, j, k: (j, k))
        dims = (((1,), (1,)), ((), ()))
    else:
        a_spec = pl.BlockSpec((tk, tm), lambda i, j, k: (k, i))
        b_spec = pl.BlockSpec((tk, tn), lambda i, j, k: (k, j))
        dims = (((0,), (0,)), ((), ()))

    if out_blocked:
        per_o = (cols // out_blocked) // tn
        out_spec = pl.BlockSpec((None, tm, tn), lambda i, j, k: (j // per_o, i, j % per_o))
        out_shape = jax.ShapeDtypeStruct((out_blocked, rows, cols // out_blocked), out_dtype)
    else:
        out_spec = pl.BlockSpec((tm, tn), lambda i, j, k: (i, j))
        out_shape = jax.ShapeDtypeStruct((rows, cols), out_dtype)

    def body(a_ref, b_ref, o_ref, *acc):
        prod = lax.dot_general(a_ref[...], b_ref[...], dims, preferred_element_type=F32)
        if nk == 1:
            o_ref[...] = prod.astype(out_dtype)
        else:
            acc_ref, = acc
            k = pl.program_id(2)

            @pl.when(k == 0)
            def _():
                acc_ref[...] = prod

            @pl.when(k > 0)
            def _():
                acc_ref[...] += prod

            @pl.when(k == nk - 1)
            def _():
                o_ref[...] = acc_ref[...].astype(out_dtype)

    scratch = [pltpu.VMEM((tm, tn), F32)] if nk > 1 else []
    return _pallas(body, name=name, grid=(ni, nj, nk), in_specs=[a_spec, b_spec], out_specs=out_spec,
                   out_shape=out_shape, operands=[a, b], scratch_shapes=scratch, rider=rider)


def _rmsnorm_fwd(x, w, name):
    s, d = x.shape
    tm = 256

    def body(x_ref, w_ref, o_ref):
        xv = x_ref[...]
        rstd = lax.rsqrt(jnp.mean(xv * xv, axis=-1, keepdims=True) + EPS)
        o_ref[...] = (xv * rstd * w_ref[...]).astype(BF16)

    return pl.pallas_call(
        body, name=name, grid=(s // tm,),
        in_specs=[pl.BlockSpec((tm, d), lambda i: (i, 0)), pl.BlockSpec((1, d), lambda i: (0, 0))],
        out_specs=pl.BlockSpec((tm, d), lambda i: (i, 0)),
        out_shape=jax.ShapeDtypeStruct((s, d), BF16), compiler_params=_params(),
    )(x, w)


def _rmsnorm_bwd(dh, x, w, dres, name, rider=None):
    s, d = x.shape
    tm = 256

    def body(dh_ref, x_ref, w_ref, dres_ref, dx_ref, dxb_ref, dw_ref):
        xv = x_ref[...]
        rstd = lax.rsqrt(jnp.mean(xv * xv, axis=-1, keepdims=True) + EPS)
        xhat = xv * rstd
        dhv = dh_ref[...]
        g = dhv * w_ref[...]
        dx = rstd * (g - xhat * jnp.mean(g * xhat, axis=-1, keepdims=True)) + dres_ref[...]
        dx_ref[...] = dx
        dxb_ref[...] = dx.astype(BF16)
        part = jnp.sum(dhv * xhat, axis=0, keepdims=True)

        @pl.when(pl.program_id(0) == 0)
        def _():
            dw_ref[...] = part

        @pl.when(pl.program_id(0) > 0)
        def _():
            dw_ref[...] += part

    row = pl.BlockSpec((tm, d), lambda i: (i, 0))
    vec = pl.BlockSpec((1, d), lambda i: (0, 0))
    return _pallas(
        body, name=name, grid=(s // tm,), in_specs=[row, row, vec, row], out_specs=[row, row, vec],
        out_shape=[jax.ShapeDtypeStruct((s, d), F32), jax.ShapeDtypeStruct((s, d), BF16),
                   jax.ShapeDtypeStruct((1, d), F32)],
        operands=[dh, x, w, dres], rider=rider)


def _rope_consts():
    lane = np.arange(LANES)
    in_head = lane % HEAD_DIM
    inv_freq = ROPE_THETA ** (-jnp.arange(0, ROT_DIM, 2, dtype=F32) / ROT_DIM)
    invf = jnp.where(jnp.asarray(in_head < ROT_DIM), jnp.tile(inv_freq, LANES // (ROT_DIM // 2)), 0.0)
    m_a = np.where(in_head < ROT_DIM // 2, -1.0, 0.0).astype(np.float32)
    m_b = np.where((in_head >= ROT_DIM // 2) & (in_head < ROT_DIM), 1.0, 0.0).astype(np.float32)
    block_diag = (lane[:, None] // HEAD_DIM == lane[None, :] // HEAD_DIM).astype(np.float32)
    return (invf.reshape(1, LANES).astype(F32), jnp.asarray(m_a).reshape(1, LANES),
            jnp.asarray(m_b).reshape(1, LANES), jnp.asarray(block_diag, dtype=BF16))


def _head_sums(v, bd):
    hi = v.astype(BF16)
    lo = (v - hi.astype(F32)).astype(BF16)
    return jnp.dot(hi, bd, preferred_element_type=F32) + jnp.dot(lo, bd, preferred_element_type=F32)


def _qk_fwd(proj, pos_col, qw2, kw2, consts, name, rider=None):
    s = proj.shape[0]
    width = 3 * N_SLOT_HEADS * HEAD_DIM
    tm = 128
    invf, m_a, m_b, bd = consts
    scale = HEAD_DIM ** -0.5

    def body(q_ref, k_ref, pos_ref, qw_ref, kw_ref, invf_ref, ma_ref, mb_ref, bd_ref, qo_ref, ko_ref):
        ang = pos_ref[...].astype(F32) * invf_ref[...]
        cos = jnp.cos(ang)
        sin = jnp.sin(ang)
        s_a = sin * ma_ref[...]
        s_b = sin * mb_ref[...]
        bdv = bd_ref[...]
        for src, w_ref, dst, sc in ((q_ref, qw_ref, qo_ref, scale), (k_ref, kw_ref, ko_ref, 1.0)):
            for cb in range(width // LANES):
                cols = slice(cb * LANES, (cb + 1) * LANES)
                t = src[:, cols]
                rstd = lax.rsqrt(_head_sums(t * t, bdv) * (1.0 / HEAD_DIM) + EPS)
                y = t * rstd * w_ref[...]
                r = y * cos + pltpu.roll(y, LANES - 8, axis=1) * s_a + pltpu.roll(y, 8, axis=1) * s_b
                dst[:, cols] = r * sc if sc != 1.0 else r

    vec = pl.BlockSpec((1, LANES), lambda i: (0, 0))
    return _pallas(
        body, name=name, grid=(s // tm,),
        in_specs=[pl.BlockSpec((tm, width), lambda i: (i, 0)), pl.BlockSpec((tm, width), lambda i: (i, 1)),
                  pl.BlockSpec((tm, 1), lambda i: (i, 0)), vec, vec, vec, vec, vec,
                  pl.BlockSpec((LANES, LANES), lambda i: (0, 0))],
        out_specs=[pl.BlockSpec((tm, width), lambda i: (i, 0))] * 2,
        out_shape=[jax.ShapeDtypeStruct((s, width), F32)] * 2,
        operands=[proj, proj, pos_col, qw2, kw2, invf, m_a, m_b, bd], rider=rider)


def _qk_bwd(dqn, dkn, dv, da, db, dgl, proj, pos_col, qw2, kw2, consts, name, rider=None):
    s = proj.shape[0]
    width = 3 * N_SLOT_HEADS * HEAD_DIM
    ch = da.shape[1]
    gate_w = dgl.shape[1]
    out_w = 3 * width + 2 * ch + gate_w
    assert out_w == proj.shape[1]
    tm = 128
    invf, m_a, m_b, bd = consts
    scale = HEAD_DIM ** -0.5

    def body(dq_ref, dk_ref, dv_ref, da_ref, db_ref, dgl_ref, q_ref, k_ref, pos_ref, qw_ref, kw_ref,
             invf_ref, ma_ref, mb_ref, bd_ref, out_ref, dqw_ref, dkw_ref):
        ang = pos_ref[...].astype(F32) * invf_ref[...]
        cos = jnp.cos(ang)
        sin = jnp.sin(ang)
        s_a = sin * ma_ref[...]
        s_b = sin * mb_ref[...]
        bdv = bd_ref[...]
        first = pl.program_id(0) == 0
        for src, dsrc, w_ref, col0, dw_ref, sc in ((q_ref, dq_ref, qw_ref, 0, dqw_ref, scale),
                                                   (k_ref, dk_ref, kw_ref, width, dkw_ref, 1.0)):
            dw_acc = jnp.zeros((1, LANES), F32)
            for cb in range(width // LANES):
                cols = slice(cb * LANES, (cb + 1) * LANES)
                t = src[:, cols]
                dr = dsrc[:, cols]
                if sc != 1.0:
                    dr = dr * sc
                dy = dr * cos + pltpu.roll(dr * s_a, 8, axis=1) + pltpu.roll(dr * s_b, LANES - 8, axis=1)
                rstd = lax.rsqrt(_head_sums(t * t, bdv) * (1.0 / HEAD_DIM) + EPS)
                xhat = t * rstd
                g = dy * w_ref[...]
                dt = rstd * (g - xhat * (_head_sums(g * xhat, bdv) * (1.0 / HEAD_DIM)))
                out_ref[:, col0 + cb * LANES: col0 + (cb + 1) * LANES] = dt.astype(BF16)
                dw_acc = dw_acc + jnp.sum(dy * xhat, axis=0, keepdims=True)
            dw_acc = dw_acc + pltpu.roll(dw_acc, HEAD_DIM, axis=1)

            @pl.when(first)
            def _(dw_ref=dw_ref, dw_acc=dw_acc):
                dw_ref[...] = dw_acc

            @pl.when(jnp.logical_not(first))
            def _(dw_ref=dw_ref, dw_acc=dw_acc):
                dw_ref[...] += dw_acc
        out_ref[:, 2 * width: 3 * width] = dv_ref[...].astype(BF16)
        out_ref[:, 3 * width: 3 * width + ch] = da_ref[...]
        out_ref[:, 3 * width + ch: 3 * width + 2 * ch] = db_ref[...]
        out_ref[:, 3 * width + 2 * ch: out_w] = dgl_ref[...]

    vec = pl.BlockSpec((1, LANES), lambda i: (0, 0))
    blk = lambda c: pl.BlockSpec((tm, width), lambda i: (i, c))
    cblk = pl.BlockSpec((tm, ch), lambda i: (i, 0))
    return _pallas(
        body, name=name, grid=(s // tm,),
        in_specs=[blk(0), blk(0), blk(0), cblk, cblk, pl.BlockSpec((tm, gate_w), lambda i: (i, 0)),
                  blk(0), blk(1), pl.BlockSpec((tm, 1), lambda i: (i, 0)), vec, vec, vec, vec, vec,
                  pl.BlockSpec((LANES, LANES), lambda i: (0, 0))],
        out_specs=[pl.BlockSpec((tm, out_w), lambda i: (i, 0)), vec, vec],
        out_shape=[jax.ShapeDtypeStruct((s, out_w), BF16)] + [jax.ShapeDtypeStruct((1, LANES), F32)] * 2,
        operands=[dqn, dkn, dv, da, db, dgl, proj, proj, pos_col, qw2, kw2, invf, m_a, m_b, bd],
        rider=rider)


def _row_chunks(n_rows, fn, chunk=256):
    def step(i, c):
        fn(pl.ds(pl.multiple_of(i * chunk, chunk), chunk))
        return c
    lax.fori_loop(0, n_rows // chunk, step, 0)


def _to_residue_major(dst, src, s, d, dst_off=0, cast=None):
    seq = s // d
    for r in range(d):
        v = src[...] if d == 1 else src[pl.ds(r, seq, stride=d), :]
        dst[dst_off + r * seq: dst_off + (r + 1) * seq, :] = v if cast is None else v.astype(cast)


def _from_residue_major(dst, src, s, d, src_off=0):
    seq = s // d
    for r in range(d):
        v = src[src_off + r * seq: src_off + (r + 1) * seq, :]
        if d == 1:
            dst[...] = v
        else:
            dst[pl.ds(r, seq, stride=d), :] = v


def _band_bias():
    qi = lax.broadcasted_iota(jnp.int32, (QBLK, KWIN), 0)
    kj = lax.broadcasted_iota(jnp.int32, (QBLK, KWIN), 1)
    return jnp.where(jnp.abs(kj - HALF_SPAN - qi) <= HALF_SPAN, 0.0, NEG_INF).astype(F32)


def _range_bias(base, seq):
    kj = lax.broadcasted_iota(jnp.int32, (1, KWIN), 1)
    lo = (base & -seq) - base + HALF_SPAN
    return jnp.where((kj >= lo) & (kj < lo + seq), 0.0, NEG_INF).astype(F32)


def _block_base(b):
    return b * QBLK if isinstance(b, int) else pl.multiple_of(b * QBLK, QBLK)


def _attn_fwd(qn, kn, proj, name, rider=None):
    s = qn.shape[0]
    n_pairs = N_SLOT_HEADS * HEAD_DIM // LANES
    v_col0 = 2 * qn.shape[1] // LANES
    nt_dims = (((1,), (1,)), ((), ()))

    def body(q_ref, k_ref, v_ref, attn_ref, lse_ref, attn_b_ref, q_rm, k_rm, v_rm, acc_rm, m_rm, l_rm,
             acc_p, m_p, l_p, m_run, l_run, acc_run, band, s_buf, m_buf):
        g = pl.program_id(1)
        zpad = jnp.zeros((HALF_SPAN, LANES), BF16)
        k_rm[0:HALF_SPAN, :] = zpad
        k_rm[s + HALF_SPAN: s + 2 * HALF_SPAN, :] = zpad
        v_rm[0:HALF_SPAN, 0:LANES] = zpad
        v_rm[s + HALF_SPAN: s + 2 * HALF_SPAN, 0:LANES] = zpad

        def ones_rows(rows):
            v_rm[pl.ds(rows.start, rows.size), LANES:2 * LANES] = jnp.ones((rows.size, LANES), BF16)

        _row_chunks(s + 2 * HALF_SPAN, ones_rows, chunk=2 * HALF_SPAN)
        band[...] = _band_bias()
        lane = lax.broadcasted_iota(jnp.int32, (QBLK, LANES), 1)
        low = lane < HEAD_DIM
        n_blk = s // QBLK

        for gi, d in enumerate(DILATIONS):
            @pl.when(g == gi)
            def _(gi=gi, d=d):
                seq = s // d
                _to_residue_major(q_rm, q_ref, s, d, cast=BF16)
                _to_residue_major(k_rm, k_ref, s, d, dst_off=HALF_SPAN, cast=BF16)
                _to_residue_major(v_rm.at[:, 0:LANES], v_ref, s, d, dst_off=HALF_SPAN, cast=BF16)

                def scores(b, slot):
                    base = _block_base(b)
                    q = q_rm[pl.ds(base, QBLK), :]
                    zero = jnp.zeros_like(q)
                    q2 = jnp.concatenate([jnp.where(low, q, zero), jnp.where(low, zero, q)], axis=0)
                    sc = lax.dot_general(q2, k_rm[pl.ds(base, KWIN), :], nt_dims, preferred_element_type=F32)
                    bias = band[...] + _range_bias(base, seq)
                    for hh in range(2):
                        rows = slice(hh * QBLK, (hh + 1) * QBLK)
                        sh = sc[rows, :] + bias
                        s_buf[slot, rows, :] = sh
                        m_buf[slot, rows, :] = jnp.broadcast_to(jnp.max(sh, axis=-1, keepdims=True), (QBLK, LANES))

                def outputs(b, slot):
                    base = _block_base(b)
                    sv = s_buf[slot]
                    mb = m_buf[slot]
                    p = jnp.exp(jnp.concatenate([sv[:, 0:LANES] - mb, sv[:, LANES:2 * LANES] - mb], axis=1))
                    pv = jnp.dot(p.astype(BF16), v_rm[pl.ds(base, KWIN), :], preferred_element_type=F32)
                    rows = pl.ds(base, QBLK)
                    acc_rm[rows, :] = jnp.where(low, pv[0:QBLK, 0:LANES], pv[QBLK:2 * QBLK, 0:LANES])
                    l_rm[rows, :] = jnp.where(low, pv[0:QBLK, LANES:2 * LANES], pv[QBLK:2 * QBLK, LANES:2 * LANES])
                    m_rm[rows, :] = jnp.where(low, mb[0:QBLK, :], mb[QBLK:2 * QBLK, :])

                scores(0, 0)

                def pair(i, carry):
                    b = 2 * i
                    outputs(b, 0)
                    scores(b + 1, 1)
                    outputs(b + 1, 1)
                    scores(b + 2, 0)
                    return carry

                lax.fori_loop(0, n_blk // 2 - 1, pair, 0)
                outputs(n_blk - 2, 0)
                scores(n_blk - 1, 1)
                outputs(n_blk - 1, 1)
                if d == 1:
                    src = (acc_rm, m_rm, l_rm)
                else:
                    for dst_, src_ in ((acc_p, acc_rm), (m_p, m_rm), (l_p, l_rm)):
                        _from_residue_major(dst_, src_, s, d)
                    src = (acc_p, m_p, l_p)

                def combine(rows):
                    a_g, m_g, l_g = src[0][rows, :], src[1][rows, :], src[2][rows, :]
                    if gi == 0:
                        m_new, l_new, a_new = m_g, l_g, a_g
                    else:
                        m_old = m_run[rows, :]
                        m_new = jnp.maximum(m_old, m_g)
                        w_old = jnp.exp(m_old - m_new)
                        w_g = jnp.exp(m_g - m_new)
                        l_new = l_run[rows, :] * w_old + l_g * w_g
                        a_new = acc_run[rows, :] * w_old + a_g * w_g
                    if gi == len(DILATIONS) - 1:
                        out = a_new / l_new
                        attn_ref[rows, :] = out
                        attn_b_ref[rows, :] = out.astype(BF16)
                        lse_ref[rows, :] = m_new + jnp.log(l_new)
                    else:
                        m_run[rows, :] = m_new
                        l_run[rows, :] = l_new
                        acc_run[rows, :] = a_new

                _row_chunks(s, combine)

    qk_spec = pl.BlockSpec((s, LANES), lambda hp, g: (0, g * n_pairs + hp))
    v_spec = pl.BlockSpec((s, LANES), lambda hp, g: (0, v_col0 + g * n_pairs + hp))
    o_spec = pl.BlockSpec((s, LANES), lambda hp, g: (0, hp))
    f32buf = pltpu.VMEM((s, LANES), F32)
    return _pallas(
        body, name=name, grid=(n_pairs, len(DILATIONS)), in_specs=[qk_spec, qk_spec, v_spec],
        out_specs=[o_spec, o_spec, o_spec],
        out_shape=[jax.ShapeDtypeStruct((s, n_pairs * LANES), F32)] * 2
        + [jax.ShapeDtypeStruct((s, n_pairs * LANES), BF16)],
        operands=[qn, kn, proj],
        scratch_shapes=[pltpu.VMEM((s, LANES), BF16), pltpu.VMEM((s + 2 * HALF_SPAN, LANES), BF16),
                        pltpu.VMEM((s + 2 * HALF_SPAN, 2 * LANES), BF16)] + [f32buf] * 9
        + [pltpu.VMEM((QBLK, KWIN), F32), pltpu.VMEM((2, 2 * QBLK, KWIN), F32),
           pltpu.VMEM((2, 2 * QBLK, LANES), F32)],
        rider=rider)


def _attn_bwd(qn, kn, proj, dattn, attn, lse, bd, name, rider=None):
    s = qn.shape[0]
    n_pairs = N_SLOT_HEADS * HEAD_DIM // LANES
    v_col0 = 2 * qn.shape[1] // LANES
    nt_dims = (((1,), (1,)), ((), ()))
    tn_dims = (((0,), (0,)), ((), ()))
    spad = s + 2 * HALF_SPAN

    def body(q_ref, k_ref, v_ref, do_ref, o_ref, lse_ref, bd_ref, dq_ref, dk_ref, dv_ref,
             q_rm, k_rm, v_rm, do_rm, lse0_rm, lse1_rm, dd0_rm, dd1_rm, dq_rm, dk_rm, dv_rm,
             lse0_p, lse1_p, dd0_p, dd1_p, band, p_buf, ds_buf):
        g = pl.program_id(1)
        zpad = jnp.zeros((HALF_SPAN, LANES), BF16)
        for buf in (k_rm, v_rm):
            buf[0:HALF_SPAN, :] = zpad
            buf[s + HALF_SPAN: spad, :] = zpad
        zf = jnp.zeros((HALF_SPAN, LANES), F32)
        for buf in (dk_rm, dv_rm):
            buf[0:HALF_SPAN, :] = zf
            buf[s + HALF_SPAN: spad, :] = zf
        band[...] = _band_bias()

        def clear(rows):
            z = jnp.zeros((rows.size, LANES), F32)
            dk_rm[pl.ds(rows.start + HALF_SPAN, rows.size), :] = z
            dv_rm[pl.ds(rows.start + HALF_SPAN, rows.size), :] = z

        _row_chunks(s, clear)

        def prepare(rows):
            lo = lax.broadcasted_iota(jnp.int32, (rows.size, LANES), 1) < HEAD_DIM
            dsum = _head_sums(do_ref[rows, :] * o_ref[rows, :], bd_ref[...])
            dswap = pltpu.roll(dsum, HEAD_DIM, axis=1)
            dd0_p[rows, :] = jnp.where(lo, dsum, dswap)
            dd1_p[rows, :] = jnp.where(lo, dswap, dsum)
            lv = lse_ref[rows, :]
            lswap = pltpu.roll(lv, HEAD_DIM, axis=1)
            lse0_p[rows, :] = jnp.where(lo, lv, lswap)
            lse1_p[rows, :] = jnp.where(lo, lswap, lv)

        @pl.when(g == 0)
        def _():
            _row_chunks(s, prepare)
        lane = lax.broadcasted_iota(jnp.int32, (QBLK, LANES), 1)
        low = lane < HEAD_DIM
        n_blk = s // QBLK

        def stacked(ref, rows):
            val = ref[rows, :]
            zero = jnp.zeros_like(val)
            return jnp.concatenate([jnp.where(low, val, zero), jnp.where(low, zero, val)], axis=0)

        for gi, d in enumerate(DILATIONS):
            @pl.when(g == gi)
            def _(d=d):
                seq = s // d
                _to_residue_major(q_rm, q_ref, s, d, cast=BF16)
                _to_residue_major(k_rm, k_ref, s, d, dst_off=HALF_SPAN, cast=BF16)
                _to_residue_major(v_rm, v_ref, s, d, dst_off=HALF_SPAN, cast=BF16)
                _to_residue_major(do_rm, do_ref, s, d, cast=BF16)
                for dst_, src_ in ((lse0_rm, lse0_p), (lse1_rm, lse1_p), (dd0_rm, dd0_p), (dd1_rm, dd1_p)):
                    _to_residue_major(dst_, src_, s, d)

                def scores(b, slot):
                    base = _block_base(b)
                    rows = pl.ds(base, QBLK)
                    win = pl.ds(base, KWIN)
                    sc = lax.dot_general(stacked(q_rm, rows), k_rm[win, :], nt_dims, preferred_element_type=F32)
                    dp = lax.dot_general(stacked(do_rm, rows), v_rm[win, :], nt_dims, preferred_element_type=F32)
                    bias = band[...] + _range_bias(base, seq)
                    for hh, (lse_r, dd_r) in enumerate(((lse0_rm, dd0_rm), (lse1_rm, dd1_rm))):
                        r = slice(hh * QBLK, (hh + 1) * QBLK)
                        lse_h = lse_r[rows, :]
                        dd_h = dd_r[rows, :]
                        sh = sc[r, :] + bias
                        p = jnp.exp(jnp.concatenate([sh[:, 0:LANES] - lse_h, sh[:, LANES:KWIN] - lse_h], axis=1))
                        dph = dp[r, :]
                        ds = p * jnp.concatenate([dph[:, 0:LANES] - dd_h, dph[:, LANES:KWIN] - dd_h], axis=1)
                        p_buf[slot, r, :] = p.astype(BF16)
                        ds_buf[slot, r, :] = ds.astype(BF16)

                def grads(b, slot):
                    base = _block_base(b)
                    rows = pl.ds(base, QBLK)
                    win = pl.ds(base, KWIN)
                    p = p_buf[slot]
                    ds = ds_buf[slot]
                    dq2 = jnp.dot(ds, k_rm[win, :], preferred_element_type=F32)
                    dq_rm[rows, :] = jnp.where(low, dq2[0:QBLK, :], dq2[QBLK:2 * QBLK, :])
                    dk_rm[win, :] += lax.dot_general(ds, stacked(q_rm, rows), tn_dims, preferred_element_type=F32)
                    dv_rm[win, :] += lax.dot_general(p, stacked(do_rm, rows), tn_dims, preferred_element_type=F32)

                scores(0, 0)

                def pair(i, carry):
                    b = 2 * i
                    grads(b, 0)
                    scores(b + 1, 1)
                    grads(b + 1, 1)
                    scores(b + 2, 0)
                    return carry

                lax.fori_loop(0, n_blk // 2 - 1, pair, 0)
                grads(n_blk - 2, 0)
                scores(n_blk - 1, 1)
                grads(n_blk - 1, 1)
                _from_residue_major(dq_ref, dq_rm, s, d)
                _from_residue_major(dk_ref, dk_rm, s, d, src_off=HALF_SPAN)
                _from_residue_major(dv_ref, dv_rm, s, d, src_off=HALF_SPAN)

    qk_spec = pl.BlockSpec((s, LANES), lambda hp, g: (0, g * n_pairs + hp))
    v_spec = pl.BlockSpec((s, LANES), lambda hp, g: (0, v_col0 + g * n_pairs + hp))
    o_spec = pl.BlockSpec((s, LANES), lambda hp, g: (0, hp))
    width = qn.shape[1]
    f32buf = pltpu.VMEM((s, LANES), F32)
    f32pad = pltpu.VMEM((spad, LANES), F32)
    return _pallas(
        body, name=name, grid=(n_pairs, len(DILATIONS)),
        in_specs=[qk_spec, qk_spec, v_spec, o_spec, o_spec, o_spec,
                  pl.BlockSpec((LANES, LANES), lambda hp, g: (0, 0))],
        out_specs=[qk_spec, qk_spec, qk_spec],
        out_shape=[jax.ShapeDtypeStruct((s, width), F32)] * 3,
        operands=[qn, kn, proj, dattn, attn, lse, bd],
        scratch_shapes=[pltpu.VMEM((s, LANES), BF16), pltpu.VMEM((spad, LANES), BF16),
                        pltpu.VMEM((spad, LANES), BF16), pltpu.VMEM((s, LANES), BF16),
                        f32buf, f32buf, f32buf, f32buf, f32buf, f32pad, f32pad,
                        f32buf, f32buf, f32buf, f32buf, pltpu.VMEM((QBLK, KWIN), F32),
                        pltpu.VMEM((2, 2 * QBLK, KWIN), BF16), pltpu.VMEM((2, 2 * QBLK, KWIN), BF16)],
        rider=rider)


CONV_PAD = 16


def _conv_fwd(proj, conv_w, conv_b, col0, name, rider=None):
    s = proj.shape[0]
    ch = conv_w.shape[1]
    nblk = ch // LANES
    a0 = col0 // LANES
    tr = 256
    shift = CONV_PAD - (CONV_WIDTH - 1) // 2

    def body(a_ref, b_ref, w_ref, bias_ref, u0_ref, uc_ref, pad):
        z = jnp.zeros((CONV_PAD, LANES), F32)
        pad[0:CONV_PAD, :] = z
        pad[s + CONV_PAD: s + 2 * CONV_PAD, :] = z

        def glu(rows):
            u0 = a_ref[rows, :] * jax.nn.sigmoid(b_ref[rows, :])
            u0_ref[rows, :] = u0
            pad[pl.ds(rows.start + CONV_PAD, rows.size), :] = u0

        _row_chunks(s, glu)
        for t in range(0, s, tr):
            acc = jnp.broadcast_to(bias_ref[...], (tr, LANES))
            for k in range(CONV_WIDTH):
                acc = acc + w_ref[k:k + 1, :] * pad[t + k + shift: t + k + shift + tr, :]
            uc_ref[t:t + tr, :] = acc

    return _pallas(
        body, name=name, grid=(nblk,),
        in_specs=[pl.BlockSpec((s, LANES), lambda c: (0, a0 + c)),
                  pl.BlockSpec((s, LANES), lambda c: (0, a0 + nblk + c)),
                  pl.BlockSpec((CONV_WIDTH, LANES), lambda c: (0, c)),
                  pl.BlockSpec((1, LANES), lambda c: (0, c))],
        out_specs=[pl.BlockSpec((s, LANES), lambda c: (0, c))] * 2,
        out_shape=[jax.ShapeDtypeStruct((s, ch), F32)] * 2, operands=[proj, proj, conv_w, conv_b],
        scratch_shapes=[pltpu.VMEM((s + 2 * CONV_PAD, LANES), F32)], rider=rider)


def _ln_silu_fwd(uc, ln_w, ln_b, name):
    s, ch = uc.shape
    tm = 256

    def body(u_ref, w_ref, b_ref, o_ref):
        u = u_ref[...]
        mu = jnp.mean(u, axis=-1, keepdims=True)
        xc = u - mu
        rstd = lax.rsqrt(jnp.mean(xc * xc, axis=-1, keepdims=True) + EPS)
        z = xc * rstd * w_ref[...] + b_ref[...]
        o_ref[...] = (z * jax.nn.sigmoid(z)).astype(BF16)

    row = pl.BlockSpec((tm, ch), lambda i: (i, 0))
    vec = pl.BlockSpec((1, ch), lambda i: (0, 0))
    return pl.pallas_call(
        body, name=name, grid=(s // tm,), in_specs=[row, vec, vec], out_specs=row,
        out_shape=jax.ShapeDtypeStruct((s, ch), BF16), compiler_params=_params(),
    )(uc, ln_w, ln_b)


def _ln_silu_bwd(du3, uc, ln_w, ln_b, name):
    s, ch = uc.shape
    tm = 256

    def body(d_ref, u_ref, w_ref, b_ref, du_ref, dw_ref, db_ref):
        u = u_ref[...]
        mu = jnp.mean(u, axis=-1, keepdims=True)
        xc = u - mu
        rstd = lax.rsqrt(jnp.mean(xc * xc, axis=-1, keepdims=True) + EPS)
        xhat = xc * rstd
        z = xhat * w_ref[...] + b_ref[...]
        sg = jax.nn.sigmoid(z)
        dz = d_ref[...] * (sg * (1.0 + z * (1.0 - sg)))
        dxh = dz * w_ref[...]
        du_ref[...] = rstd * (dxh - jnp.mean(dxh, axis=-1, keepdims=True)
                              - xhat * jnp.mean(dxh * xhat, axis=-1, keepdims=True))
        pw = jnp.sum(dz * xhat, axis=0, keepdims=True)
        pb = jnp.sum(dz, axis=0, keepdims=True)
        first = pl.program_id(0) == 0

        @pl.when(first)
        def _():
            dw_ref[...] = pw
            db_ref[...] = pb

        @pl.when(jnp.logical_not(first))
        def _():
            dw_ref[...] += pw
            db_ref[...] += pb

    row = pl.BlockSpec((tm, ch), lambda i: (i, 0))
    vec = pl.BlockSpec((1, ch), lambda i: (0, 0))
    return pl.pallas_call(
        body, name=name, grid=(s // tm,), in_specs=[row, row, vec, vec], out_specs=[row, vec, vec],
        out_shape=[jax.ShapeDtypeStruct((s, ch), F32), jax.ShapeDtypeStruct((1, ch), F32),
                   jax.ShapeDtypeStruct((1, ch), F32)],
        compiler_params=_params(),
    )(du3, uc, ln_w, ln_b)


def _conv_bwd(duc, u0, proj, conv_w, col0, name, rider=None):
    s = proj.shape[0]
    ch = conv_w.shape[1]
    nblk = ch // LANES
    a0 = col0 // LANES
    tr = 256
    half = (CONV_WIDTH - 1) // 2
    shift = CONV_PAD - half

    def body(duc_ref, u0_ref, a_ref, b_ref, w_ref, da_ref, db_ref, dw_ref, dbias_ref, pad_d, pad_u):
        z = jnp.zeros((CONV_PAD, LANES), F32)
        for buf in (pad_d, pad_u):
            buf[0:CONV_PAD, :] = z
            buf[s + CONV_PAD: s + 2 * CONV_PAD, :] = z

        def fill(rows):
            dst = pl.ds(rows.start + CONV_PAD, rows.size)
            pad_d[dst, :] = duc_ref[rows, :]
            pad_u[dst, :] = u0_ref[rows, :]

        _row_chunks(s, fill)
        dw_acc = [jnp.zeros((8, LANES), F32) for _ in range(CONV_WIDTH)]
        dbias_acc = jnp.zeros((8, LANES), F32)
        for t in range(0, s, tr):
            d_t = duc_ref[t:t + tr, :]
            dbias_acc = dbias_acc + jnp.sum(d_t.reshape(tr // 8, 8, LANES), axis=0)
            du0 = jnp.zeros((tr, LANES), F32)
            for k in range(CONV_WIDTH):
                du0 = du0 + w_ref[k:k + 1, :] * pad_d[t - k + half + CONV_PAD: t - k + half + CONV_PAD + tr, :]
                prod = d_t * pad_u[t + k + shift: t + k + shift + tr, :]
                dw_acc[k] = dw_acc[k] + jnp.sum(prod.reshape(tr // 8, 8, LANES), axis=0)
            av = a_ref[t:t + tr, :]
            sg = jax.nn.sigmoid(b_ref[t:t + tr, :])
            da_ref[t:t + tr, :] = (du0 * sg).astype(BF16)
            db_ref[t:t + tr, :] = (du0 * av * sg * (1.0 - sg)).astype(BF16)
        for k in range(CONV_WIDTH):
            dw_ref[k:k + 1, :] = jnp.sum(dw_acc[k], axis=0, keepdims=True)
        dbias_ref[...] = jnp.sum(dbias_acc, axis=0, keepdims=True)

    col = lambda off: pl.BlockSpec((s, LANES), lambda c: (0, off + c))
    return _pallas(
        body, name=name, grid=(nblk,),
        in_specs=[col(0), col(0), col(a0), col(a0 + nblk),
                  pl.BlockSpec((CONV_WIDTH, LANES), lambda c: (0, c))],
        out_specs=[col(0), col(0), pl.BlockSpec((CONV_WIDTH, LANES), lambda c: (0, c)),
                   pl.BlockSpec((1, LANES), lambda c: (0, c))],
        out_shape=[jax.ShapeDtypeStruct((s, ch), BF16)] * 2
        + [jax.ShapeDtypeStruct((CONV_WIDTH, ch), F32), jax.ShapeDtypeStruct((1, ch), F32)],
        operands=[duc, u0, proj, proj, conv_w],
        scratch_shapes=[pltpu.VMEM((s + 2 * CONV_PAD, LANES), F32)] * 2, rider=rider)


GATE_BLK = 512


def _gate_fwd(proj, bg, y_a, y_b, col0, name):
    s, d = y_a.shape
    tm = 256
    g0 = col0 // GATE_BLK
    nb = d // GATE_BLK

    def body(ga_ref, gb_ref, ba_ref, bb_ref, ya_ref, yb_ref, o_ref):
        g_a = jax.nn.sigmoid(ga_ref[...] + ba_ref[...])
        g_b = jax.nn.sigmoid(gb_ref[...] + bb_ref[...])
        o_ref[...] = (g_a * ya_ref[...] + g_b * yb_ref[...]).astype(BF16)

    act = pl.BlockSpec((tm, GATE_BLK), lambda i, j: (i, j))
    return pl.pallas_call(
        body, name=name, grid=(s // tm, nb),
        in_specs=[pl.BlockSpec((tm, GATE_BLK), lambda i, j: (i, g0 + j)),
                  pl.BlockSpec((tm, GATE_BLK), lambda i, j: (i, g0 + nb + j)),
                  pl.BlockSpec((None, 1, GATE_BLK), lambda i, j: (0, 0, j)),
                  pl.BlockSpec((None, 1, GATE_BLK), lambda i, j: (1, 0, j)), act, act],
        out_specs=act, out_shape=jax.ShapeDtypeStruct((s, d), BF16), compiler_params=_params(),
    )(proj, proj, bg, bg, y_a, y_b)


def _gate_bwd(d_mixed, proj, bg, y_a, y_b, col0, name, rider=None):
    s, d = y_a.shape
    tm = 256
    half = d // 2
    assert col0 % half == 0
    c0 = col0 // half

    def body(dm_ref, a0_ref, a1_ref, b0_ref, b1_ref, bias_ref, ya_ref, yb_ref, dgl_ref, dya_ref, dyb_ref, db_ref):
        dm = dm_ref[...]
        parts = []
        for br, (lo_ref, hi_ref, y_ref, dy_ref) in enumerate(((a0_ref, a1_ref, ya_ref, dya_ref),
                                                              (b0_ref, b1_ref, yb_ref, dyb_ref))):
            logits = jnp.concatenate([lo_ref[...], hi_ref[...]], axis=1)
            gate = jax.nn.sigmoid(logits + bias_ref[br])
            dy_ref[...] = (dm * gate).astype(BF16)
            dgl = dm * y_ref[...] * gate * (1.0 - gate)
            dgl_ref[:, br * d:(br + 1) * d] = dgl.astype(BF16)
            parts.append(jnp.sum(dgl, axis=0, keepdims=True))
        part = jnp.concatenate(parts, axis=0)
        first = pl.program_id(0) == 0

        @pl.when(first)
        def _():
            db_ref[...] = part

        @pl.when(jnp.logical_not(first))
        def _():
            db_ref[...] += part

    row = pl.BlockSpec((tm, d), lambda i: (i, 0))
    logit_blk = lambda k: pl.BlockSpec((tm, half), functools.partial(lambda i, k: (i, c0 + k), k=k))
    return _pallas(
        body, name=name, grid=(s // tm,),
        in_specs=[row, logit_blk(0), logit_blk(1), logit_blk(2), logit_blk(3),
                  pl.BlockSpec((2, 1, d), lambda i: (0, 0, 0)), row, row],
        out_specs=[pl.BlockSpec((tm, 2 * d), lambda i: (i, 0)), row, row, pl.BlockSpec((2, d), lambda i: (0, 0))],
        out_shape=[jax.ShapeDtypeStruct((s, 2 * d), BF16), jax.ShapeDtypeStruct((s, d), BF16),
                   jax.ShapeDtypeStruct((s, d), BF16), jax.ShapeDtypeStruct((2, d), F32)],
        operands=[d_mixed, proj, proj, proj, proj, bg, y_a, y_b], rider=rider)


def _ffn_in_swiglu(h2, w_blocked, name):
    s, k = h2.shape
    nblk, _, tn = w_blocked.shape
    ff = nblk // 2 * tn
    tm = 512

    def body(a_ref, wg_ref, wu_ref, g_ref, u_ref, act_ref):
        a = a_ref[...]
        gt = jnp.dot(a, wg_ref[...], preferred_element_type=F32)
        up = jnp.dot(a, wu_ref[...], preferred_element_type=F32)
        g_ref[...] = gt
        u_ref[...] = up
        act_ref[...] = (gt * jax.nn.sigmoid(gt) * up).astype(BF16)

    out = pl.BlockSpec((tm, tn), lambda i, j: (i, j))
    return pl.pallas_call(
        body, name=name, grid=(s // tm, nblk // 2),
        in_specs=[pl.BlockSpec((tm, k), lambda i, j: (i, 0)),
                  pl.BlockSpec((None, k, tn), lambda i, j: (j, 0, 0)),
                  pl.BlockSpec((None, k, tn), lambda i, j: (nblk // 2 + j, 0, 0))],
        out_specs=[out, out, out],
        out_shape=[jax.ShapeDtypeStruct((s, ff), F32), jax.ShapeDtypeStruct((s, ff), F32),
                   jax.ShapeDtypeStruct((s, ff), BF16)],
        compiler_params=_params(),
    )(h2, w_blocked, w_blocked)


def _swiglu_bwd(gate, up, d_act, name, rider=None):
    s, ff = gate.shape
    tm = 256

    def body(g_ref, u_ref, d_ref, o_ref):
        gt = g_ref[...]
        sg = jax.nn.sigmoid(gt)
        dv = d_ref[...]
        o_ref[:, 0:ff] = (dv * u_ref[...] * (sg * (1.0 + gt * (1.0 - sg)))).astype(BF16)
        o_ref[:, ff:2 * ff] = (dv * gt * sg).astype(BF16)

    row = pl.BlockSpec((tm, ff), lambda i: (i, 0))
    return _pallas(
        body, name=name, grid=(s // tm,), in_specs=[row, row, row],
        out_specs=pl.BlockSpec((tm, 2 * ff), lambda i: (i, 0)),
        out_shape=jax.ShapeDtypeStruct((s, 2 * ff), BF16), operands=[gate, up, d_act], rider=rider)


def _out_proj_rmsnorm(mixed, w_out, x, norm_w, name):
    s, k = mixed.shape
    d = w_out.shape[1]
    tm = 512

    def body(a_ref, w_ref, x_ref, nw_ref, x1_ref, h2_ref):
        x1 = x_ref[...] + jnp.dot(a_ref[...], w_ref[...], preferred_element_type=F32)
        x1_ref[...] = x1
        rstd = lax.rsqrt(jnp.mean(x1 * x1, axis=-1, keepdims=True) + EPS)
        h2_ref[...] = (x1 * rstd * nw_ref[...]).astype(BF16)

    row = pl.BlockSpec((tm, d), lambda i: (i, 0))
    return pl.pallas_call(
        body, name=name, grid=(s // tm,),
        in_specs=[pl.BlockSpec((tm, k), lambda i: (i, 0)), pl.BlockSpec((k, d), lambda i: (0, 0)), row,
                  pl.BlockSpec((1, d), lambda i: (0, 0))],
        out_specs=[row, row],
        out_shape=[jax.ShapeDtypeStruct((s, d), F32), jax.ShapeDtypeStruct((s, d), BF16)],
        compiler_params=_params(),
    )(mixed, w_out, x, norm_w)


def _ffn_out_loss(act, w_ffn_out, x1, target, name):
    s, k = act.shape
    d = w_ffn_out.shape[1]
    tm = 512

    def body(a_ref, w_ref, x1_ref, t_ref, dy_ref, dyb_ref, loss_ref, acc):
        y = x1_ref[...] + jnp.dot(a_ref[...], w_ref[...], preferred_element_type=F32)
        diff = y - t_ref[...]
        dy = diff * (1.0 / d)
        dy_ref[...] = dy
        dyb_ref[...] = dy.astype(BF16)
        part = jnp.sum((diff * diff).reshape(tm // 8, 8, d), axis=0)
        i = pl.program_id(0)

        @pl.when(i == 0)
        def _():
            acc[...] = part

        @pl.when(i > 0)
        def _():
            acc[...] += part

        @pl.when(i == pl.num_programs(0) - 1)
        def _():
            loss_ref[...] = (0.5 / d) * jnp.sum(jnp.sum(acc[...], axis=1, keepdims=True), axis=0, keepdims=True)

    row = pl.BlockSpec((tm, d), lambda i: (i, 0))
    return pl.pallas_call(
        body, name=name, grid=(s // tm,),
        in_specs=[pl.BlockSpec((tm, k), lambda i: (i, 0)), pl.BlockSpec((k, d), lambda i: (0, 0)), row, row],
        out_specs=[row, row, pl.BlockSpec((1, 1), lambda i: (0, 0))],
        out_shape=[jax.ShapeDtypeStruct((s, d), F32), jax.ShapeDtypeStruct((s, d), BF16),
                   jax.ShapeDtypeStruct((1, 1), F32)],
        scratch_shapes=[pltpu.VMEM((8, d), F32)], compiler_params=_params(),
    )(act, w_ffn_out, x1, target)


LATE_GATHER = ("w_o_attn", "w_pw_conv", "w_out", "w_ffn_in", "w_ffn_out")
EARLY_REDUCE = LATE_GATHER


def _blocks_by_half(g):
    if g.ndim == 2:
        g = g.reshape(N_CHIPS, g.shape[0] // N_CHIPS, g.shape[1])
    return g.reshape(N_CHIPS, 2, g.shape[1] // 2, g.shape[2])


def _forward_backward(x, pos_col, target, wts, late_bufs, pos_arr):
    wts = dict(wts)
    consts = _rope_consts()
    bd = consts[3]
    qw2 = jnp.tile(wts["q_norm_w"], (1, LANES // HEAD_DIM))
    kw2 = jnp.tile(wts["k_norm_w"], (1, LANES // HEAD_DIM))
    qkv_w = 3 * N_SLOT_HEADS * HEAD_DIM
    conv_col0 = 3 * qkv_w
    ch = wts["conv_w"].shape[1]
    gate_col0 = conv_col0 + 2 * ch

    h = _rmsnorm_fwd(x, wts["norm1_w"], "rms1_fwd")
    late = dict(zip(LATE_GATHER, late_bufs))
    quarter = late["w_ffn_in"].shape[2] // 4
    proj, (late["w_o_attn"], late["w_pw_conv"], late["w_out"], late["w_ffn_in"]) = _matmul(
        h, wts["w_in"], mode="nn", tm=512, tn=1920, tk=1024, out_dtype=F32, name="mm_proj", b_blocked=True,
        rider=_gather_ici_rider([late["w_o_attn"], late["w_pw_conv"], late["w_out"], late["w_ffn_in"]], [],
                                row_ranges=[None, None, None, (0, quarter)]))
    (qn, kn), (late["w_ffn_in"],) = _qk_fwd(
        proj, pos_col, qw2, kw2, consts, "qk_fwd",
        rider=_gather_ici_rider([late["w_ffn_in"]], [], row_ranges=[(quarter, 2 * quarter)]))
    (attn, lse, attn_b), (late["w_ffn_in"], late["w_ffn_out"]) = _attn_fwd(
        qn, kn, proj, "attn_fwd",
        rider=_gather_ici_rider([late["w_ffn_in"], late["w_ffn_out"]], [],
                                row_ranges=[(3 * quarter, quarter), None]))
    (u0, uc), late_bufs = _conv_fwd(proj, wts["conv_w"], wts["conv_b"], conv_col0, "conv_fwd",
                                    rider=_gather_forward_rider([late[n] for n in LATE_GATHER]))
    for n, buf in zip(LATE_GATHER, late_bufs):
        full = buf.reshape(N_CHIPS, -1, buf.shape[3])
        wts[n] = full.reshape(-1, full.shape[2]) if n in ROW_SHARDED else full
    y_a = _matmul(attn_b, wts["w_o_attn"], mode="nn", tm=1024, tn=256, tk=512, out_dtype=F32, name="mm_ya",
                  b_blocked=True)
    u3 = _ln_silu_fwd(uc, wts["conv_ln_w"], wts["conv_ln_b"], "ln_fwd")
    y_b = _matmul(u3, wts["w_pw_conv"], mode="nn", tm=1024, tn=256, tk=512, out_dtype=F32, name="mm_yb",
                  b_blocked=True)
    mixed = _gate_fwd(proj, wts["b_gate"], y_a, y_b, gate_col0, "gate_fwd")
    x1, h2 = _out_proj_rmsnorm(mixed, wts["w_out"], x, wts["norm2_w"], "mm_x1_rms2")
    gate, up, act = _ffn_in_swiglu(h2, wts["w_ffn_in"], "mm_gu_swiglu")
    dy, dy_b16, loss = _ffn_out_loss(act, wts["w_ffn_out"], x1, target, "mm_x2_loss")

    g = {}
    by_chip = {}

    def pair_add(n, blocks, received):
        return _add_own_half(blocks, received, pos_arr, f"grads_pair_add_{n}")

    d_act = _matmul(dy_b16, wts["w_ffn_out"], mode="nt", tm=512, tn=1408, tk=1024, out_dtype=F32, name="mm_dact")
    g_ffn_out = _blocks_by_half(
        _matmul(act, dy_b16, mode="tn", tm=1408, tn=1024, tk=2048, out_dtype=F32, name="mm_dwffnout"))
    dgu, (received,) = _swiglu_bwd(gate, up, d_act, "swiglu_bwd",
                                   rider=_pair_exchange_rider([g_ffn_out], halved=True))
    to_send, own = pair_add("w_ffn_out", g_ffn_out, received)
    dh2, (by_chip["w_ffn_out"],) = _matmul(
        dgu, wts["w_ffn_in"], mode="nt", tm=1024, tn=1024, tk=1408, out_dtype=F32, name="mm_dh2", b_blocked=True,
        rider=_chip_exchange_rider([to_send], [own]))
    g_ffn_in = _blocks_by_half(_matmul(h2, dgu, mode="tn", tm=512, tn=1408, tk=2048, out_dtype=F32,
                                       name="mm_dwffnin", out_blocked=N_CHIPS))
    dx1, dx1_b16, g["norm2_w"] = _rmsnorm_bwd(dh2, x1, wts["norm2_w"], dy, "rms2_bwd")
    d_mixed = _matmul(dx1_b16, wts["w_out"], mode="nt", tm=512, tn=1024, tk=1024, out_dtype=F32, name="mm_dmixed")
    g["w_out"] = _matmul(mixed, dx1_b16, mode="tn", tm=512, tn=1024, tk=2048, out_dtype=F32, name="mm_dwout")
    (dgl, dy_a, dy_b, g["b_gate"]), (received,) = _gate_bwd(
        d_mixed, proj, wts["b_gate"], y_a, y_b, gate_col0, "gate_bwd",
        rider=_pair_exchange_rider([g_ffn_in], halved=True))
    ffn_in_to_send, ffn_in_own = pair_add("w_ffn_in", g_ffn_in, received)
    dattn = _matmul(dy_a, wts["w_o_attn"], mode="nt", tm=1024, tn=512, tk=256, out_dtype=F32, name="mm_dattn",
                    b_blocked=True)
    g["w_o_attn"] = _matmul(attn_b, dy_a, mode="tn", tm=512, tn=256, tk=2048, out_dtype=F32, name="mm_dwo",
                            out_blocked=N_CHIPS)
    du3 = _matmul(dy_b, wts["w_pw_conv"], mode="nt", tm=1024, tn=512, tk=256, out_dtype=F32, name="mm_du3",
                  b_blocked=True)
    g["w_pw_conv"] = _matmul(u3, dy_b, mode="tn", tm=512, tn=256, tk=2048, out_dtype=F32, name="mm_dwpw",
                             out_blocked=N_CHIPS)
    duc, g["conv_ln_w"], g["conv_ln_b"] = _ln_silu_bwd(du3, uc, wts["conv_ln_w"], wts["conv_ln_b"], "ln_bwd")

    small3 = ("w_out", "w_o_attn", "w_pw_conv")
    g_small3 = [_blocks_by_half(g.pop(n)) for n in small3]
    (da, db, g["conv_w"], g["conv_b"]), received = _conv_bwd(
        duc, u0, proj, wts["conv_w"], conv_col0, "conv_bwd", rider=_pair_exchange_rider(g_small3, halved=True))
    sums3 = [pair_add(n, gb, rv) for n, gb, rv in zip(small3, g_small3, received)]
    (dqn, dkn, dv), (by_chip["w_ffn_in"],) = _attn_bwd(
        qn, kn, proj, dattn, attn, lse, bd, "attn_bwd",
        rider=_chip_exchange_rider([ffn_in_to_send], [ffn_in_own]))
    (dproj, dqw, dkw), exchanged3 = _qk_bwd(
        dqn, dkn, dv, da, db, dgl, proj, pos_col, qw2, kw2, consts, "qk_bwd",
        rider=_chip_exchange_rider([s[0] for s in sums3], [s[1] for s in sums3]))
    by_chip.update(zip(small3, exchanged3))
    halves = [_sum_chips(by_chip[n], pos_arr, f"grads_chip_sum_{n}") for n in EARLY_REDUCE]
    g["q_norm_w"] = dqw[:, :HEAD_DIM]
    g["k_norm_w"] = dkw[:, :HEAD_DIM]

    c = pos_arr[0]
    rh = h.shape[1] // 2
    h_sibling = lax.dynamic_slice_in_dim(h, (1 - c) * rh, rh, axis=1)
    h_own = lax.dynamic_slice_in_dim(h, c * rh, rh, axis=1)
    g_sibling, shards = _matmul(h_sibling, dproj, mode="tn", tm=rh, tn=1920, tk=2048, out_dtype=F32,
                                name="mm_dwin_sibling", out_blocked=N_CHIPS, rider=_pair_gather_rider(halves))
    reduced = dict(zip(EARLY_REDUCE, shards))
    g_own, from_sibling = _matmul(h_own, dproj, mode="tn", tm=rh, tn=1920, tk=2048, out_dtype=F32,
                                  name="mm_dwin_own", out_blocked=N_CHIPS,
                                  rider=_pair_exchange_rider([g_sibling], halved=False))
    to_send, own = _add_own_half(g_own, from_sibling[0], pos_arr, "grads_pair_add_w_in")
    in_flight = _chip_exchange_start(to_send, own, "grads_w_in_exchange_start")
    dh = _matmul(dproj, wts["w_in"], mode="nt", tm=1024, tn=1024, tk=1920, out_dtype=F32, name="mm_dh",
                 b_blocked=True)
    grad_x, _, g["norm1_w"] = _rmsnorm_bwd(dh, x, wts["norm1_w"], dx1, "rms1_bwd")
    return loss, grad_x, g, reduced, in_flight


def _mesh_pos():
    return lax.axis_index("x"), lax.axis_index("y"), lax.axis_index("c")


def _other_chips(x, y):
    return [(1 - x, y), (x, 1 - y), (1 - x, 1 - y)]


def _cast_into_slot(shard, chip_arr, dtype, name):
    r, c = shard.shape
    tr = r // 2 if r % 32 == 0 else r

    def body(chip_ref, s_ref, o_ref):
        del chip_ref
        o_ref[...] = s_ref[...].astype(dtype)

    return pl.pallas_call(
        body, name=name,
        grid_spec=pltpu.PrefetchScalarGridSpec(
            num_scalar_prefetch=1, grid=(r // tr,),
            in_specs=[pl.BlockSpec((tr, c), lambda i, chip_ref: (i, 0))],
            out_specs=pl.BlockSpec((None, tr, c), lambda i, chip_ref: (chip_ref[0], i, 0))),
        out_shape=jax.ShapeDtypeStruct((N_CHIPS, r, c), dtype), compiler_params=_params(),
    )(chip_arr, shard)


def _allgather_inplace(big, small, name):
    nb, ns = len(big), len(small)
    n = nb + ns

    def body(*refs):
        bufs = refs[n:2 * n]
        send_sems, recv_sems, fsend_sems, frecv_sems = refs[2 * n:]
        x, y, c = _mesh_pos()
        me = 2 * x + y
        chips = _other_chips(x, y)

        def part(a, slot, half):
            return bufs[a].at[slot, half] if a < nb else bufs[a].at[slot]

        sends = []
        for a in range(n):
            for k, (px, py) in enumerate(chips):
                cp = pltpu.make_async_remote_copy(
                    src_ref=part(a, me, c), dst_ref=part(a, me, c), send_sem=send_sems.at[a, k],
                    recv_sem=recv_sems.at[a, k], device_id=(px, py, c), device_id_type=MESH)
                cp.start()
                sends.append(cp)
        for a in range(n):
            for k, (px, py) in enumerate(chips):
                slot = 2 * px + py
                pltpu.make_async_remote_copy(
                    src_ref=part(a, slot, c), dst_ref=part(a, slot, c), send_sem=send_sems.at[a, k],
                    recv_sem=recv_sems.at[a, k], device_id=(px, py, c), device_id_type=MESH).wait_recv()
                if a < nb:
                    fwd = pltpu.make_async_remote_copy(
                        src_ref=part(a, slot, c), dst_ref=part(a, slot, c), send_sem=fsend_sems.at[a, k],
                        recv_sem=frecv_sems.at[a, k], device_id=(x, y, 1 - c), device_id_type=MESH)
                    fwd.start()
                    sends.append(fwd)
        for a in range(nb):
            for k, (px, py) in enumerate(chips):
                slot = 2 * px + py
                pltpu.make_async_remote_copy(
                    src_ref=part(a, slot, 1 - c), dst_ref=part(a, slot, 1 - c), send_sem=fsend_sems.at[a, k],
                    recv_sem=frecv_sems.at[a, k], device_id=(x, y, 1 - c), device_id_type=MESH).wait_recv()
        for cp in sends:
            cp.wait_send()

    ops = list(big) + list(small)
    return pl.pallas_call(
        body, name=name, in_specs=[ANY] * n, out_specs=[ANY] * n,
        out_shape=[jax.ShapeDtypeStruct(o.shape, o.dtype) for o in ops],
        input_output_aliases={i: i for i in range(n)},
        scratch_shapes=[pltpu.SemaphoreType.DMA((n, 3)), pltpu.SemaphoreType.DMA((n, 3)),
                        pltpu.SemaphoreType.DMA((nb, 3)), pltpu.SemaphoreType.DMA((nb, 3))],
    )(*ops)


def _comm_call(rider, name):
    def body():
        pass

    return _pallas(body, name=name, grid=(1,), in_specs=[], out_specs=[], out_shape=[], operands=[],
                   rider=rider)[1]


def _gather_ici_rider(big, small, row_ranges=None):
    nb = len(big)
    n = nb + len(small)
    row_ranges = row_ranges or [None] * nb

    def copies(bufs, sems):
        x, y, c = _mesh_pos()
        me = 2 * x + y

        def part(a, slot):
            if a >= nb:
                return bufs[a].at[slot]
            if row_ranges[a] is None:
                return bufs[a].at[slot, c]
            return bufs[a].at[slot, c, pl.ds(*row_ranges[a])]
        out = []
        for a in range(n):
            for k, (px, py) in enumerate(_other_chips(x, y)):
                send = functools.partial(
                    pltpu.make_async_remote_copy,
                    src_ref=part(a, me), dst_ref=part(a, me), send_sem=sems[0].at[a, k],
                    recv_sem=sems[1].at[a, k], device_id=(px, py, c), device_id_type=MESH)
                recv = functools.partial(
                    pltpu.make_async_remote_copy,
                    src_ref=part(a, 2 * px + py), dst_ref=part(a, 2 * px + py), send_sem=sems[0].at[a, k],
                    recv_sem=sems[1].at[a, k], device_id=(px, py, c), device_id_type=MESH)
                out.append((send, recv))
        return out

    def start(r_in, r_out, sems):
        for send, _ in copies(r_out, sems):
            send().start()

    def wait(r_in, r_out, sems):
        cps = copies(r_out, sems)
        for _, recv in cps:
            recv().wait_recv()
        for send, _ in cps:
            send().wait_send()

    ops = list(big) + list(small)
    return _Rider(ops, [jax.ShapeDtypeStruct(o.shape, o.dtype) for o in ops], {i: i for i in range(n)},
                  [pltpu.SemaphoreType.DMA((n, 3)), pltpu.SemaphoreType.DMA((n, 3))], start, wait)


def _gather_forward_rider(big):
    n = len(big)

    def copies(bufs, sems):
        x, y, c = _mesh_pos()
        out = []
        for a in range(n):
            for k, (px, py) in enumerate(_other_chips(x, y)):
                slot = 2 * px + py
                send = functools.partial(
                    pltpu.make_async_remote_copy,
                    src_ref=bufs[a].at[slot, c], dst_ref=bufs[a].at[slot, c], send_sem=sems[0].at[a, k],
                    recv_sem=sems[1].at[a, k], device_id=(x, y, 1 - c), device_id_type=MESH)
                recv = functools.partial(
                    pltpu.make_async_remote_copy,
                    src_ref=bufs[a].at[slot, 1 - c], dst_ref=bufs[a].at[slot, 1 - c], send_sem=sems[0].at[a, k],
                    recv_sem=sems[1].at[a, k], device_id=(x, y, 1 - c), device_id_type=MESH)
                out.append((send, recv))
        return out

    def start(r_in, r_out, sems):
        for send, _ in copies(r_out, sems):
            send().start()

    def wait(r_in, r_out, sems):
        cps = copies(r_out, sems)
        for _, recv in cps:
            recv().wait_recv()
        for send, _ in cps:
            send().wait_send()

    return _Rider(big, [jax.ShapeDtypeStruct(o.shape, o.dtype) for o in big], {i: i for i in range(n)},
                  [pltpu.SemaphoreType.DMA((n, 3)), pltpu.SemaphoreType.DMA((n, 3))], start, wait)


def _pair_exchange_rider(gs, halved):
    n = len(gs)

    def copies(r_in, r_out, sems):
        x, y, c = _mesh_pos()
        return [pltpu.make_async_remote_copy(
            src_ref=r_in[a].at[:, 1 - c] if halved else r_in[a], dst_ref=r_out[a], send_sem=sems[0].at[a],
            recv_sem=sems[1].at[a], device_id=(x, y, 1 - c), device_id_type=MESH) for a in range(n)]

    def start(r_in, r_out, sems):
        for cp in copies(r_in, r_out, sems):
            cp.start()

    def wait(r_in, r_out, sems):
        for cp in copies(r_in, r_out, sems):
            cp.wait()

    return _Rider(gs, [jax.ShapeDtypeStruct((g.shape[0],) + g.shape[-2:], g.dtype) for g in gs], {},
                  [pltpu.SemaphoreType.DMA((n,)), pltpu.SemaphoreType.DMA((n,))], start, wait)


def _chip_exchange_rider(to_send, by_chip, row_range=None):
    n = len(to_send)

    def copies(r_in, r_out, sems):
        x, y, c = _mesh_pos()
        me = 2 * x + y
        rows = (lambda ref: ref) if row_range is None else (lambda ref: ref.at[pl.ds(*row_range)])
        out = []
        for a in range(n):
            for k, (px, py) in enumerate(_other_chips(x, y)):
                send = functools.partial(
                    pltpu.make_async_remote_copy,
                    src_ref=rows(r_in[a].at[2 * px + py]), dst_ref=rows(r_out[a].at[me]),
                    send_sem=sems[0].at[a, k], recv_sem=sems[1].at[a, k], device_id=(px, py, c),
                    device_id_type=MESH)
                recv = functools.partial(
                    pltpu.make_async_remote_copy,
                    src_ref=rows(r_in[a].at[me]), dst_ref=rows(r_out[a].at[2 * px + py]),
                    send_sem=sems[0].at[a, k], recv_sem=sems[1].at[a, k], device_id=(px, py, c),
                    device_id_type=MESH)
                out.append((send, recv))
        return out

    def start(r_in, r_out, sems):
        for send, _ in copies(r_in, r_out, sems):
            send().start()

    def wait(r_in, r_out, sems):
        cps = copies(r_in, r_out, sems)
        for _, recv in cps:
            recv().wait_recv()
        for send, _ in cps:
            send().wait_send()

    return _Rider(list(to_send) + list(by_chip), [jax.ShapeDtypeStruct(b.shape, b.dtype) for b in by_chip],
                  {n + i: i for i in range(n)},
                  [pltpu.SemaphoreType.DMA((n, 3)), pltpu.SemaphoreType.DMA((n, 3))], start, wait)


HBM = pl.BlockSpec(memory_space=pltpu.HBM)
SEM = pl.BlockSpec(memory_space=pltpu.SEMAPHORE)


def _chip_exchange_start(to_send, by_chip, name):
    def body(send_ref, buf_ref, send_sems, recv_sems, send_thru, buf_thru):
        x, y, c = _mesh_pos()
        me = 2 * x + y
        for k, (px, py) in enumerate(_other_chips(x, y)):
            pltpu.make_async_remote_copy(
                src_ref=send_ref.at[2 * px + py], dst_ref=buf_ref.at[me], send_sem=send_sems.at[k],
                recv_sem=recv_sems.at[k], device_id=(px, py, c), device_id_type=MESH).start()

    return pl.pallas_call(
        body, name=name,
        out_shape=(pltpu.SemaphoreType.DMA((3,)), pltpu.SemaphoreType.DMA((3,)),
                   pltpu.HBM(to_send.shape, to_send.dtype), pltpu.HBM(by_chip.shape, by_chip.dtype)),
        in_specs=(HBM, HBM), out_specs=(SEM, SEM, HBM, HBM), input_output_aliases={0: 2, 1: 3},
        compiler_params=pltpu.CompilerParams(has_side_effects=pltpu.SideEffectType.DATAFLOW_SIDE_EFFECTING),
    )(pltpu.with_memory_space_constraint(to_send, pltpu.HBM), pltpu.with_memory_space_constraint(by_chip, pltpu.HBM))


def _chip_exchange_wait(send_sems, recv_sems, send_thru, buf_thru, after, name):
    n_after = len(after)

    def body(send_ref, buf_ref, send_sems, recv_sems, *rest):
        x, y, c = _mesh_pos()
        me = 2 * x + y
        for k, (px, py) in enumerate(_other_chips(x, y)):
            cp = pltpu.make_async_remote_copy(
                src_ref=send_ref.at[me], dst_ref=buf_ref.at[2 * px + py], send_sem=send_sems.at[k],
                recv_sem=recv_sems.at[k], device_id=(px, py, c), device_id_type=MESH)
            cp.wait_send()
            cp.wait_recv()

    return pl.pallas_call(
        body, name=name,
        out_shape=(pltpu.HBM(send_thru.shape, send_thru.dtype), pltpu.HBM(buf_thru.shape, buf_thru.dtype)),
        in_specs=(HBM, HBM, SEM, SEM) + (ANY,) * n_after, out_specs=(HBM, HBM), input_output_aliases={0: 0, 1: 1},
        compiler_params=pltpu.CompilerParams(has_side_effects=pltpu.SideEffectType.DATAFLOW_SIDE_EFFECTING),
    )(send_thru, buf_thru, send_sems, recv_sems, *after)[1]


def _pair_gather_rider(bufs):
    n = len(bufs)

    def copies(r_out, sems):
        x, y, c = _mesh_pos()
        out = []
        for a in range(n):
            send = functools.partial(
                    pltpu.make_async_remote_copy,
                src_ref=r_out[a].at[c], dst_ref=r_out[a].at[c], send_sem=sems[0].at[a],
                recv_sem=sems[1].at[a], device_id=(x, y, 1 - c), device_id_type=MESH)
            recv = functools.partial(
                    pltpu.make_async_remote_copy,
                src_ref=r_out[a].at[1 - c], dst_ref=r_out[a].at[1 - c], send_sem=sems[0].at[a],
                recv_sem=sems[1].at[a], device_id=(x, y, 1 - c), device_id_type=MESH)
            out.append((send, recv))
        return out

    def start(r_in, r_out, sems):
        for send, _ in copies(r_out, sems):
            send().start()

    def wait(r_in, r_out, sems):
        cps = copies(r_out, sems)
        for _, recv in cps:
            recv().wait_recv()
        for send, _ in cps:
            send().wait_send()

    return _Rider(bufs, [jax.ShapeDtypeStruct(b.shape, b.dtype) for b in bufs], {i: i for i in range(n)},
                  [pltpu.SemaphoreType.DMA((n,)), pltpu.SemaphoreType.DMA((n,))], start, wait)


def _add_own_half(g, recv, pos_arr, name):
    nb, rh, cols = g.shape[0], g.shape[-2], g.shape[-1]

    def body(pos_ref, g_ref, r_ref, send_ref, own_ref):
        s = (g_ref[...] + r_ref[...]).astype(BF16)
        send_ref[...] = s

        @pl.when(pl.program_id(0) == pos_ref[1])
        def _():
            own_ref[...] = s

    blk = pl.BlockSpec((None, rh, cols), lambda j, pos_ref: (j, 0, 0))
    g_spec = blk if g.ndim == 3 else pl.BlockSpec((None, None, rh, cols),
                                                   lambda j, pos_ref: (j, pos_ref[0], 0, 0))
    shape = jax.ShapeDtypeStruct((nb, rh, cols), BF16)
    return pl.pallas_call(
        body, name=name,
        grid_spec=pltpu.PrefetchScalarGridSpec(
            num_scalar_prefetch=1, grid=(nb,), in_specs=[g_spec, blk],
            out_specs=[blk, pl.BlockSpec((None, rh, cols), lambda j, pos_ref: (pos_ref[1], 0, 0))]),
        out_shape=[shape, shape], compiler_params=_params(),
    )(pos_arr, g, recv)


def _sum_chips(gath, pos_arr, name):
    nb, rh, cols = gath.shape

    def body(pos_ref, a_ref, b_ref, c_ref, d_ref, o_ref):
        del pos_ref
        o_ref[...] = ((a_ref[...].astype(F32) + b_ref[...].astype(F32)) + c_ref[...].astype(F32)) \
            + d_ref[...].astype(F32)

    tr = rh // 2 if (rh // 2) % 16 == 0 else rh
    specs = [pl.BlockSpec((None, tr, cols), functools.partial(lambda i, pos_ref, j: (j, i, 0), j=j))
             for j in range(nb)]
    return pl.pallas_call(
        body, name=name,
        grid_spec=pltpu.PrefetchScalarGridSpec(
            num_scalar_prefetch=1, grid=(rh // tr,), in_specs=specs,
            out_specs=pl.BlockSpec((None, tr, cols), lambda i, pos_ref: (pos_ref[0], i, 0))),
        out_shape=jax.ShapeDtypeStruct((2, rh, cols), F32), compiler_params=_params(),
    )(pos_arr, gath, gath, gath, gath)


def _small_allreduce(v, name, rider=None):
    n = v.shape[0]
    n_dev = 8

    def body(v_ref, o_ref, buf, send_sems, recv_sems):
        x, y, c = _mesh_pos()
        me = 4 * x + 2 * y + c
        buf[me] = v_ref[...]
        sends = []
        peers = []
        for r in range(1, n_dev):
            px = 1 - x if r & 4 else x
            py = 1 - y if r & 2 else y
            pc = 1 - c if r & 1 else c
            peers.append((px, py, pc))
            cp = pltpu.make_async_remote_copy(
                src_ref=v_ref, dst_ref=buf.at[me], send_sem=send_sems.at[r - 1],
                recv_sem=recv_sems.at[r - 1], device_id=(px, py, pc), device_id_type=MESH)
            cp.start()
            sends.append(cp)
        for r, (px, py, pc) in enumerate(peers):
            pltpu.make_async_remote_copy(
                src_ref=v_ref, dst_ref=buf.at[4 * px + 2 * py + pc], send_sem=send_sems.at[r],
                recv_sem=recv_sems.at[r], device_id=(px, py, pc), device_id_type=MESH).wait_recv()
        for cp in sends:
            cp.wait_send()
        acc = buf[0]
        for i in range(1, n_dev):
            acc = acc + buf[i]
        o_ref[...] = acc

    whole = pl.BlockSpec(v.shape, lambda i: (0, 0))
    return _pallas(
        body, name=name, grid=(1,), in_specs=[whole], out_specs=whole,
        out_shape=jax.ShapeDtypeStruct(v.shape, v.dtype), operands=[v],
        scratch_shapes=[pltpu.VMEM((n_dev, n, LANES), F32), pltpu.SemaphoreType.DMA((n_dev - 1,)),
                        pltpu.SemaphoreType.DMA((n_dev - 1,))],
        rider=rider)


def _adamw_math(w, g, m, v):
    m = ADAM_B1 * m + (1.0 - ADAM_B1) * g
    v = ADAM_B2 * v + (1.0 - ADAM_B2) * (g * g)
    m_hat = m / (1.0 - ADAM_B1 ** ADAM_STEP)
    v_hat = v / (1.0 - ADAM_B2 ** ADAM_STEP)
    delta = -ADAM_LR * (m_hat / (jnp.sqrt(v_hat) + ADAM_EPS) + ADAM_WD * w)
    return delta, m, v


def _adamw(w, g, m, v, name):
    r, c = w.shape
    tr = 128 if r % 128 == 0 else 64
    assert r % tr == 0

    def body(w_ref, g_ref, m_ref, v_ref, go_ref, d_ref, mo_ref, vo_ref):
        gv = g_ref[...]
        d, mn, vn = _adamw_math(w_ref[...], gv, m_ref[...], v_ref[...])
        go_ref[...] = gv
        d_ref[...] = d
        mo_ref[...] = mn
        vo_ref[...] = vn

    blk = pl.BlockSpec((tr, c), lambda i: (i, 0))
    return _pallas(body, name=name, grid=(r // tr,), in_specs=[blk] * 4, out_specs=[blk] * 4,
                   out_shape=[jax.ShapeDtypeStruct((r, c), F32)] * 4, operands=[w, g, m, v])


def _adamw_small(ws, gs, ms, vs, name):
    n = len(ws)

    def body(*refs):
        w_r, g_r, m_r, v_r = refs[:n], refs[n:2 * n], refs[2 * n:3 * n], refs[3 * n:4 * n]
        d_o, m_o, v_o = refs[4 * n:5 * n], refs[5 * n:6 * n], refs[6 * n:7 * n]
        for i in range(n):
            d, mn, vn = _adamw_math(w_r[i][...], g_r[i][...], m_r[i][...], v_r[i][...])
            d_o[i][...] = d
            m_o[i][...] = mn
            v_o[i][...] = vn

    specs = [pl.BlockSpec(w.shape, lambda i: (0, 0)) for w in ws]
    shapes = [jax.ShapeDtypeStruct(w.shape, F32) for w in ws]
    outs = pl.pallas_call(
        body, name=name, grid=(1,), in_specs=specs * 4, out_specs=specs * 3, out_shape=shapes * 3,
        compiler_params=_params(),
    )(*ws, *gs, *ms, *vs)
    return outs[:n], outs[n:2 * n], outs[2 * n:]


BIG = ("w_in", "w_o_attn", "w_pw_conv", "w_out", "w_ffn_in", "w_ffn_out")
ROW_SHARDED = ("w_out", "w_ffn_out")
SMALL = ("norm1_w", "b_gate", "q_norm_w", "k_norm_w", "conv_w", "conv_b", "conv_ln_w", "conv_ln_b", "norm2_w")
ORDER = ("norm1_w", "w_in", "b_gate", "q_norm_w", "k_norm_w", "w_o_attn", "conv_w", "conv_b", "conv_ln_w",
         "conv_ln_b", "w_pw_conv", "w_out", "norm2_w", "w_ffn_in", "w_ffn_out")
PACK_TILE = 8 * LANES


def _pack_small(parts):
    rows = []
    for p in parts:
        flat = p.reshape(-1)
        pad = (-flat.shape[0]) % PACK_TILE
        rows.append(jnp.pad(flat, (0, pad)).reshape(-1, LANES))
    return jnp.concatenate(rows, axis=0)


def _unpack_small(packed, shapes):
    out, row = [], 0
    for shp in shapes:
        size = int(np.prod(shp))
        nrow = -(-size // PACK_TILE) * (PACK_TILE // LANES)
        out.append(packed[row:row + nrow].reshape(-1)[:size].reshape(shp))
        row += nrow
    return out


def kernel(x, positions, norm1_w, w_in, b_gate, q_norm_w, k_norm_w, w_o_attn, conv_w, conv_b, conv_ln_w, conv_ln_b, w_pw_conv, w_out, norm2_w, w_ffn_in, w_ffn_out, loss_target, m_norm1_w, m_w_in, m_b_gate, m_q_norm_w, m_k_norm_w, m_w_o_attn, m_conv_w, m_conv_b, m_conv_ln_w, m_conv_ln_b, m_w_pw_conv, m_w_out, m_norm2_w, m_w_ffn_in, m_w_ffn_out, v_norm1_w, v_w_in, v_b_gate, v_q_norm_w, v_k_norm_w, v_w_o_attn, v_conv_w, v_conv_b, v_conv_ln_w, v_conv_ln_b, v_w_pw_conv, v_w_out, v_norm2_w, v_w_ffn_in, v_w_ffn_out):
    w = dict(norm1_w=norm1_w, w_in=w_in, b_gate=b_gate, q_norm_w=q_norm_w, k_norm_w=k_norm_w, w_o_attn=w_o_attn,
             conv_w=conv_w, conv_b=conv_b, conv_ln_w=conv_ln_w, conv_ln_b=conv_ln_b, w_pw_conv=w_pw_conv,
             w_out=w_out, norm2_w=norm2_w, w_ffn_in=w_ffn_in, w_ffn_out=w_ffn_out)
    m = dict(norm1_w=m_norm1_w, w_in=m_w_in, b_gate=m_b_gate, q_norm_w=m_q_norm_w, k_norm_w=m_k_norm_w,
             w_o_attn=m_w_o_attn, conv_w=m_conv_w, conv_b=m_conv_b, conv_ln_w=m_conv_ln_w,
             conv_ln_b=m_conv_ln_b, w_pw_conv=m_w_pw_conv, w_out=m_w_out, norm2_w=m_norm2_w,
             w_ffn_in=m_w_ffn_in, w_ffn_out=m_w_ffn_out)
    v = dict(norm1_w=v_norm1_w, w_in=v_w_in, b_gate=v_b_gate, q_norm_w=v_q_norm_w, k_norm_w=v_k_norm_w,
             w_o_attn=v_w_o_attn, conv_w=v_conv_w, conv_b=v_conv_b, conv_ln_w=v_conv_ln_w,
             conv_ln_b=v_conv_ln_b, w_pw_conv=v_w_pw_conv, w_out=v_w_out, norm2_w=v_norm2_w,
             w_ffn_in=v_w_ffn_in, w_ffn_out=v_w_ffn_out)
    cx, cy, cc = _mesh_pos()
    chip = 2 * cx + cy

    chip_arr = chip.reshape(1).astype(jnp.int32)
    pos_arr = jnp.stack([cc, chip]).astype(jnp.int32)
    bufs = {}
    for n in BIG:
        buf = _cast_into_slot(w[n][0], chip_arr, BF16, f"cast_{n}")
        bufs[n] = buf.reshape(N_CHIPS, 2, buf.shape[1] // 2, buf.shape[2])
    small_bufs = [_cast_into_slot(w[n][0], chip_arr, F32, f"slot_{n}") for n in ("conv_w", "b_gate")]
    w_in_buf, conv_w_buf, b_gate_buf = _allgather_inplace([bufs["w_in"]], small_bufs, "allgather_w_in")
    wts = dict(w_in=w_in_buf.reshape(N_CHIPS, -1, w_in_buf.shape[3]),
               conv_w=conv_w_buf.transpose(1, 0, 2).reshape(CONV_WIDTH, -1),
               b_gate=b_gate_buf.transpose(1, 0, 2).reshape(2, 1, -1),
               norm1_w=norm1_w, q_norm_w=q_norm_w, k_norm_w=k_norm_w, conv_b=conv_b, conv_ln_w=conv_ln_w,
               conv_ln_b=conv_ln_b, norm2_w=norm2_w)

    loss, grad_x, g, reduced, w_in_in_flight = _forward_backward(
        x[0], positions.reshape(-1, 1), loss_target[0], wts, [bufs[n] for n in LATE_GATHER], pos_arr)
    grads = {n: b.reshape(-1, b.shape[2]) for n, b in reduced.items()}

    delta, new_m, new_v = {}, {}, {}
    for n in EARLY_REDUCE:
        grads[n], delta[n], new_m[n], new_v[n] = _adamw(w[n][0], grads[n], m[n][0], v[n][0], f"adamw_{n}")
    small_parts = [loss] + [g[n] for n in SMALL]
    small_shapes = [p.shape for p in small_parts]
    summed = _small_allreduce(_pack_small(small_parts), "small_allreduce")
    reduced = _unpack_small(summed, small_shapes)
    loss_total = reduced[0].reshape(())
    for n, r in zip(SMALL, reduced[1:]):
        grads[n] = r
    ch_shard = conv_w.shape[2]
    grads["conv_w"] = lax.dynamic_slice_in_dim(grads["conv_w"], chip * ch_shard, ch_shard, axis=1)
    d_shard = b_gate.shape[2]
    grads["b_gate"] = lax.dynamic_slice_in_dim(grads["b_gate"], chip * d_shard, d_shard, axis=1)

    by_chip_w_in = _chip_exchange_wait(*w_in_in_flight, after=[delta[n] for n in EARLY_REDUCE] + [summed],
                                       name="grads_w_in_exchange_wait")
    half_w_in = _sum_chips(by_chip_w_in, pos_arr, "grads_chip_sum_w_in")
    (shard_w_in,) = _comm_call(_pair_gather_rider([half_w_in]), "grads_pair_gather_w_in")
    grads["w_in"], delta["w_in"], new_m["w_in"], new_v["w_in"] = _adamw(
        w["w_in"][0], shard_w_in.reshape(-1, shard_w_in.shape[2]), m["w_in"][0], v["w_in"][0], "adamw_w_in")
    flat2 = lambda a: a.reshape(-1, a.shape[-1])
    d_s, m_s, v_s = _adamw_small([flat2(w[n]) for n in SMALL], [flat2(grads[n]) for n in SMALL],
                                 [flat2(m[n]) for n in SMALL], [flat2(v[n]) for n in SMALL], "adamw_small")
    for i, n in enumerate(SMALL):
        delta[n], new_m[n], new_v[n] = d_s[i], m_s[i], v_s[i]

    shaped = lambda d, n: d[n].reshape(w[n].shape)
    return (loss_total, grad_x[None], *[shaped(grads, n) for n in ORDER], *[shaped(delta, n) for n in ORDER],
            *[shaped(new_m, n) for n in ORDER], *[shaped(new_v, n) for n in ORDER])
```

```python
import functools

import numpy as np
import jax
import jax.numpy as jnp
from jax import lax
from jax.experimental import pallas as pl
from jax.experimental.pallas import tpu as pltpu

F32 = jnp.float32
BF16 = jnp.bfloat16
MESH = pl.DeviceIdType.MESH
ANY = pl.BlockSpec(memory_space=pl.ANY)

HEAD_DIM = 64
N_SLOT_HEADS = 8
DILATIONS = (1, 4, 16)
HALF_SPAN = 64
ROPE_THETA = 500000.0
ROT_DIM = 16
CONV_WIDTH = 31
EPS = 1e-6
NEG_INF = -1e30
ADAM_LR, ADAM_B1, ADAM_B2, ADAM_EPS, ADAM_WD, ADAM_STEP = 0.001, 0.9, 0.999, 1e-08, 0.01, 10

LANES = 128
QBLK = 128
KWIN = QBLK + 2 * HALF_SPAN
VMEM_LIMIT = 48 * 1024 * 1024
N_CHIPS = 4


def _params(**kw):
    return pltpu.CompilerParams(vmem_limit_bytes=VMEM_LIMIT, **kw)


class _Rider:
    def __init__(self, operands, out_shapes, aliases, scratch, start, wait):
        self.operands, self.out_shapes, self.aliases = list(operands), list(out_shapes), dict(aliases)
        self.scratch, self.start, self.wait = list(scratch), start, wait


def _pallas(body, *, name, grid, in_specs, out_specs, out_shape, operands, scratch_shapes=(), aliases=None,
            rider=None, after=()):
    single = not isinstance(out_specs, (list, tuple))
    out_specs_l = [out_specs] if single else list(out_specs)
    out_shape_l = [out_shape] if single else list(out_shape)
    aliases = dict(aliases or {})
    if rider is None:
        n_main = len(in_specs)

        def ordered(*refs):
            body(*refs[:n_main], *refs[n_main + len(after):])

        res = pl.pallas_call(
            ordered if after else body, name=name, grid=grid, in_specs=list(in_specs) + [ANY] * len(after),
            out_specs=out_specs_l, out_shape=out_shape_l, scratch_shapes=list(scratch_shapes),
            input_output_aliases=aliases, compiler_params=_params(),
        )(*operands, *after)
        return res[0] if single else res
    assert not after
    n_in, n_rin = len(in_specs), len(rider.operands)
    n_out, n_rout = len(out_specs_l), len(rider.out_shapes)
    n_sc = len(scratch_shapes)

    def wrapped(*refs):
        main_in, r_in = refs[:n_in], refs[n_in:n_in + n_rin]
        o0 = n_in + n_rin
        main_out, r_out = refs[o0:o0 + n_out], refs[o0 + n_out:o0 + n_out + n_rout]
        s0 = o0 + n_out + n_rout
        main_sc, r_sc = refs[s0:s0 + n_sc], refs[s0 + n_sc:]
        ids = [pl.program_id(d) for d in range(len(grid))]
        first = functools.reduce(jnp.logical_and, [i == 0 for i in ids])
        last = functools.reduce(jnp.logical_and, [i == n - 1 for i, n in zip(ids, grid)])

        @pl.when(first)
        def _():
            rider.start(r_in, r_out, r_sc)

        body(*main_in, *main_out, *main_sc)

        @pl.when(last)
        def _():
            rider.wait(r_in, r_out, r_sc)

    for src, dst in rider.aliases.items():
        aliases[n_in + src] = n_out + dst
    res = pl.pallas_call(
        wrapped, name=name, grid=grid, in_specs=list(in_specs) + [ANY] * n_rin,
        out_specs=out_specs_l + [ANY] * n_rout, out_shape=out_shape_l + rider.out_shapes,
        scratch_shapes=list(scratch_shapes) + rider.scratch, input_output_aliases=aliases,
        compiler_params=_params(),
    )(*operands, *rider.operands)
    main = res[:n_out]
    return (main[0] if single else main), res[n_out:]


def _matmul(a, b, *, mode, tm, tn, tk, out_dtype, name, b_blocked=False,
            out_blocked=None, rider=None, after=()):
    a_shape = a.shape
    if mode == "nn":
        m_dim, k_dim = a_shape
        n_dim = b.shape[0] * b.shape[2] if b_blocked else b.shape[1]
        rows, cols, red = m_dim, n_dim, k_dim
    elif mode == "nt":
        m_dim, n_dim = a_shape
        k_dim = b.shape[1] if b_blocked else b.shape[0]
        rows, cols, red = m_dim, k_dim, n_dim
    else:
        m_dim, k_dim = a_shape
        n_dim = b.shape[1]
        rows, cols, red = k_dim, n_dim, m_dim
    assert rows % tm == 0 and cols % tn == 0 and red % tk == 0, (name, rows, cols, red)
    ni, nj, nk = rows // tm, cols // tn, red // tk

    if mode == "nn":
        a_spec = pl.BlockSpec((tm, tk), lambda i, j, k: (i, k))
        if b_blocked:
            per = b.shape[2] // tn
            b_spec = pl.BlockSpec((None, tk, tn), lambda i, j, k: (j // per, k, j % per))
        else:
            b_spec = pl.BlockSpec((tk, tn), lambda i, j, k: (k, j))
        dims = (((1,), (0,)), ((), ()))
    elif mode == "nt":
        a_spec = pl.BlockSpec((tm, tk), lambda i, j, k: (i, k))
        if b_blocked:
            per = b.shape[2] // tk
            b_spec = pl.BlockSpec((None, tn, tk), lambda i, j, k: (k // per, j, k % per))
        else:
            b_spec = pl.BlockSpec((tn, tk), lambda i, j, k: (j, k))
        dims = (((1,), (1,)), ((), ()))
    else:
        a_spec = pl.BlockSpec((tk, tm), lambda i, j, k: (k, i))
        b_spec = pl.BlockSpec((tk, tn), lambda i, j, k: (k, j))
        dims = (((0,), (0,)), ((), ()))

    if out_blocked:
        per_o = (cols // out_blocked) // tn
        out_spec = pl.BlockSpec((None, tm, tn), lambda i, j, k: (j // per_o, i, j % per_o))
        out_shape = jax.ShapeDtypeStruct((out_blocked, rows, cols // out_blocked), out_dtype)
    else:
        out_spec = pl.BlockSpec((tm, tn), lambda i, j, k: (i, j))
        out_shape = jax.ShapeDtypeStruct((rows, cols), out_dtype)

    def body(a_ref, b_ref, o_ref, *acc):
        prod = lax.dot_general(a_ref[...], b_ref[...], dims, preferred_element_type=F32)
        if nk == 1:
            o_ref[...] = prod.astype(out_dtype)
        else:
            acc_ref, = acc
            k = pl.program_id(2)

            @pl.when(k == 0)
            def _():
                acc_ref[...] = prod

            @pl.when(k > 0)
            def _():
                acc_ref[...] += prod

            @pl.when(k == nk - 1)
            def _():
                o_ref[...] = acc_ref[...].astype(out_dtype)

    scratch = [pltpu.VMEM((tm, tn), F32)] if nk > 1 else []
    return _pallas(body, name=name, grid=(ni, nj, nk), in_specs=[a_spec, b_spec], out_specs=out_spec,
                   out_shape=out_shape, operands=[a, b], scratch_shapes=scratch, rider=rider, after=after)


def _rmsnorm_fwd(x, w, name):
    s, d = x.shape
    tm = 256

    def body(x_ref, w_ref, o_ref):
        xv = x_ref[...]
        rstd = lax.rsqrt(jnp.mean(xv * xv, axis=-1, keepdims=True) + EPS)
        o_ref[...] = (xv * rstd * w_ref[...]).astype(BF16)

    return pl.pallas_call(
        body, name=name, grid=(s // tm,),
        in_specs=[pl.BlockSpec((tm, d), lambda i: (i, 0)), pl.BlockSpec((1, d), lambda i: (0, 0))],
        out_specs=pl.BlockSpec((tm, d), lambda i: (i, 0)),
        out_shape=jax.ShapeDtypeStruct((s, d), BF16), compiler_params=_params(),
    )(x, w)


def _rmsnorm_bwd(dh, x, w, dres, name, rider=None):
    s, d = x.shape
    tm = 256

    def body(dh_ref, x_ref, w_ref, dres_ref, dx_ref, dxb_ref, dw_ref):
        xv = x_ref[...]
        rstd = lax.rsqrt(jnp.mean(xv * xv, axis=-1, keepdims=True) + EPS)
        xhat = xv * rstd
        dhv = dh_ref[...]
        g = dhv * w_ref[...]
        dx = rstd * (g - xhat * jnp.mean(g * xhat, axis=-1, keepdims=True)) + dres_ref[...]
        dx_ref[...] = dx
        dxb_ref[...] = dx.astype(BF16)
        part = jnp.sum(dhv * xhat, axis=0, keepdims=True)

        @pl.when(pl.program_id(0) == 0)
        def _():
            dw_ref[...] = part

        @pl.when(pl.program_id(0) > 0)
        def _():
            dw_ref[...] += part

    row = pl.BlockSpec((tm, d), lambda i: (i, 0))
    vec = pl.BlockSpec((1, d), lambda i: (0, 0))
    return _pallas(
        body, name=name, grid=(s // tm,), in_specs=[row, row, vec, row], out_specs=[row, row, vec],
        out_shape=[jax.ShapeDtypeStruct((s, d), F32), jax.ShapeDtypeStruct((s, d), BF16),
                   jax.ShapeDtypeStruct((1, d), F32)],
        operands=[dh, x, w, dres], rider=rider)


def _rope_consts():
    lane = np.arange(LANES)
    in_head = lane % HEAD_DIM
    inv_freq = ROPE_THETA ** (-jnp.arange(0, ROT_DIM, 2, dtype=F32) / ROT_DIM)
    invf = jnp.where(jnp.asarray(in_head < ROT_DIM), jnp.tile(inv_freq, LANES // (ROT_DIM // 2)), 0.0)
    m_a = np.where(in_head < ROT_DIM // 2, -1.0, 0.0).astype(np.float32)
    m_b = np.where((in_head >= ROT_DIM // 2) & (in_head < ROT_DIM), 1.0, 0.0).astype(np.float32)
    block_diag = (lane[:, None] // HEAD_DIM == lane[None, :] // HEAD_DIM).astype(np.float32)
    return (invf.reshape(1, LANES).astype(F32), jnp.asarray(m_a).reshape(1, LANES),
            jnp.asarray(m_b).reshape(1, LANES), jnp.asarray(block_diag, dtype=BF16))


def _head_sums(v, bd):
    hi = v.astype(BF16)
    lo = (v - hi.astype(F32)).astype(BF16)
    return jnp.dot(hi, bd, preferred_element_type=F32) + jnp.dot(lo, bd, preferred_element_type=F32)


def _qk_fwd(proj, pos_col, qw2, kw2, consts, name, rider=None):
    s = proj.shape[0]
    width = 3 * N_SLOT_HEADS * HEAD_DIM
    tm = 128
    invf, m_a, m_b, bd = consts
    scale = HEAD_DIM ** -0.5

    def body(q_ref, k_ref, pos_ref, qw_ref, kw_ref, invf_ref, ma_ref, mb_ref, bd_ref, qo_ref, ko_ref):
        ang = pos_ref[...].astype(F32) * invf_ref[...]
        cos = jnp.cos(ang)
        sin = jnp.sin(ang)
        s_a = sin * ma_ref[...]
        s_b = sin * mb_ref[...]
        bdv = bd_ref[...]
        for src, w_ref, dst, sc in ((q_ref, qw_ref, qo_ref, scale), (k_ref, kw_ref, ko_ref, 1.0)):
            for cb in range(width // LANES):
                cols = slice(cb * LANES, (cb + 1) * LANES)
                t = src[:, cols]
                rstd = lax.rsqrt(_head_sums(t * t, bdv) * (1.0 / HEAD_DIM) + EPS)
                y = t * rstd * w_ref[...]
                r = y * cos + pltpu.roll(y, LANES - 8, axis=1) * s_a + pltpu.roll(y, 8, axis=1) * s_b
                dst[:, cols] = r * sc if sc != 1.0 else r

    vec = pl.BlockSpec((1, LANES), lambda i: (0, 0))
    return _pallas(
        body, name=name, grid=(s // tm,),
        in_specs=[pl.BlockSpec((tm, width), lambda i: (i, 0)), pl.BlockSpec((tm, width), lambda i: (i, 1)),
                  pl.BlockSpec((tm, 1), lambda i: (i, 0)), vec, vec, vec, vec, vec,
                  pl.BlockSpec((LANES, LANES), lambda i: (0, 0))],
        out_specs=[pl.BlockSpec((tm, width), lambda i: (i, 0))] * 2,
        out_shape=[jax.ShapeDtypeStruct((s, width), F32)] * 2,
        operands=[proj, proj, pos_col, qw2, kw2, invf, m_a, m_b, bd], rider=rider)


def _qk_bwd(dqn, dkn, dv, da, db, dgl, proj, pos_col, qw2, kw2, consts, name, rider=None):
    s = proj.shape[0]
    width = 3 * N_SLOT_HEADS * HEAD_DIM
    ch = da.shape[1]
    gate_w = dgl.shape[1]
    out_w = 3 * width + 2 * ch + gate_w
    assert out_w == proj.shape[1]
    tm = 128
    invf, m_a, m_b, bd = consts
    scale = HEAD_DIM ** -0.5

    def body(dq_ref, dk_ref, dv_ref, da_ref, db_ref, dgl_ref, q_ref, k_ref, pos_ref, qw_ref, kw_ref,
             invf_ref, ma_ref, mb_ref, bd_ref, out_ref, dqw_ref, dkw_ref):
        ang = pos_ref[...].astype(F32) * invf_ref[...]
        cos = jnp.cos(ang)
        sin = jnp.sin(ang)
        s_a = sin * ma_ref[...]
        s_b = sin * mb_ref[...]
        bdv = bd_ref[...]
        first = pl.program_id(0) == 0
        for src, dsrc, w_ref, col0, dw_ref, sc in ((q_ref, dq_ref, qw_ref, 0, dqw_ref, scale),
                                                   (k_ref, dk_ref, kw_ref, width, dkw_ref, 1.0)):
            dw_acc = jnp.zeros((1, LANES), F32)
            for cb in range(width // LANES):
                cols = slice(cb * LANES, (cb + 1) * LANES)
                t = src[:, cols]
                dr = dsrc[:, cols]
                if sc != 1.0:
                    dr = dr * sc
                dy = dr * cos + pltpu.roll(dr * s_a, 8, axis=1) + pltpu.roll(dr * s_b, LANES - 8, axis=1)
                rstd = lax.rsqrt(_head_sums(t * t, bdv) * (1.0 / HEAD_DIM) + EPS)
                xhat = t * rstd
                g = dy * w_ref[...]
                dt = rstd * (g - xhat * (_head_sums(g * xhat, bdv) * (1.0 / HEAD_DIM)))
                out_ref[:, col0 + cb * LANES: col0 + (cb + 1) * LANES] = dt.astype(BF16)
                dw_acc = dw_acc + jnp.sum(dy * xhat, axis=0, keepdims=True)
            dw_acc = dw_acc + pltpu.roll(dw_acc, HEAD_DIM, axis=1)

            @pl.when(first)
            def _(dw_ref=dw_ref, dw_acc=dw_acc):
                dw_ref[...] = dw_acc

            @pl.when(jnp.logical_not(first))
            def _(dw_ref=dw_ref, dw_acc=dw_acc):
                dw_ref[...] += dw_acc
        out_ref[:, 2 * width: 3 * width] = dv_ref[...].astype(BF16)
        out_ref[:, 3 * width: 3 * width + ch] = da_ref[...]
        out_ref[:, 3 * width + ch: 3 * width + 2 * ch] = db_ref[...]
        out_ref[:, 3 * width + 2 * ch: out_w] = dgl_ref[...]

    vec = pl.BlockSpec((1, LANES), lambda i: (0, 0))
    blk = lambda c: pl.BlockSpec((tm, width), lambda i: (i, c))
    cblk = pl.BlockSpec((tm, ch), lambda i: (i, 0))
    return _pallas(
        body, name=name, grid=(s // tm,),
        in_specs=[blk(0), blk(0), blk(0), cblk, cblk, pl.BlockSpec((tm, gate_w), lambda i: (i, 0)),
                  blk(0), blk(1), pl.BlockSpec((tm, 1), lambda i: (i, 0)), vec, vec, vec, vec, vec,
                  pl.BlockSpec((LANES, LANES), lambda i: (0, 0))],
        out_specs=[pl.BlockSpec((tm, out_w), lambda i: (i, 0)), vec, vec],
        out_shape=[jax.ShapeDtypeStruct((s, out_w), BF16)] + [jax.ShapeDtypeStruct((1, LANES), F32)] * 2,
        operands=[dqn, dkn, dv, da, db, dgl, proj, proj, pos_col, qw2, kw2, invf, m_a, m_b, bd],
        rider=rider)


def _row_chunks(n_rows, fn, chunk=256):
    def step(i, c):
        fn(pl.ds(pl.multiple_of(i * chunk, chunk), chunk))
        return c
    lax.fori_loop(0, n_rows // chunk, step, 0)


def _to_residue_major(dst, src, s, d, dst_off=0, cast=None):
    seq = s // d
    for r in range(d):
        v = src[...] if d == 1 else src[pl.ds(r, seq, stride=d), :]
        dst[dst_off + r * seq: dst_off + (r + 1) * seq, :] = v if cast is None else v.astype(cast)


def _from_residue_major(dst, src, s, d, src_off=0):
    seq = s // d
    for r in range(d):
        v = src[src_off + r * seq: src_off + (r + 1) * seq, :]
        if d == 1:
            dst[...] = v
        else:
            dst[pl.ds(r, seq, stride=d), :] = v


def _band_bias():
    qi = lax.broadcasted_iota(jnp.int32, (QBLK, KWIN), 0)
    kj = lax.broadcasted_iota(jnp.int32, (QBLK, KWIN), 1)
    return jnp.where(jnp.abs(kj - HALF_SPAN - qi) <= HALF_SPAN, 0.0, NEG_INF).astype(F32)


def _range_bias(base, seq):
    kj = lax.broadcasted_iota(jnp.int32, (1, KWIN), 1)
    lo = (base & -seq) - base + HALF_SPAN
    return jnp.where((kj >= lo) & (kj < lo + seq), 0.0, NEG_INF).astype(F32)


def _block_base(b):
    return b * QBLK if isinstance(b, int) else pl.multiple_of(b * QBLK, QBLK)


def _attn_fwd(qn, kn, proj, name, rider=None):
    s = qn.shape[0]
    n_pairs = N_SLOT_HEADS * HEAD_DIM // LANES
    v_col0 = 2 * qn.shape[1] // LANES
    nt_dims = (((1,), (1,)), ((), ()))

    def body(q_ref, k_ref, v_ref, attn_ref, lse_ref, attn_b_ref, q_rm, k_rm, v_rm, acc_rm, m_rm, l_rm,
             acc_p, m_p, l_p, m_run, l_run, acc_run, band, s_buf, m_buf):
        g = pl.program_id(1)
        zpad = jnp.zeros((HALF_SPAN, LANES), BF16)
        k_rm[0:HALF_SPAN, :] = zpad
        k_rm[s + HALF_SPAN: s + 2 * HALF_SPAN, :] = zpad
        v_rm[0:HALF_SPAN, 0:LANES] = zpad
        v_rm[s + HALF_SPAN: s + 2 * HALF_SPAN, 0:LANES] = zpad

        def ones_rows(rows):
            v_rm[pl.ds(rows.start, rows.size), LANES:2 * LANES] = jnp.ones((rows.size, LANES), BF16)

        _row_chunks(s + 2 * HALF_SPAN, ones_rows, chunk=2 * HALF_SPAN)
        band[...] = _band_bias()
        lane = lax.broadcasted_iota(jnp.int32, (QBLK, LANES), 1)
        low = lane < HEAD_DIM
        n_blk = s // QBLK

        for gi, d in enumerate(DILATIONS):
            @pl.when(g == gi)
            def _(gi=gi, d=d):
                seq = s // d
                _to_residue_major(q_rm, q_ref, s, d, cast=BF16)
                _to_residue_major(k_rm, k_ref, s, d, dst_off=HALF_SPAN, cast=BF16)
                _to_residue_major(v_rm.at[:, 0:LANES], v_ref, s, d, dst_off=HALF_SPAN, cast=BF16)

                def scores(b, slot):
                    base = _block_base(b)
                    q = q_rm[pl.ds(base, QBLK), :]
                    zero = jnp.zeros_like(q)
                    q2 = jnp.concatenate([jnp.where(low, q, zero), jnp.where(low, zero, q)], axis=0)
                    sc = lax.dot_general(q2, k_rm[pl.ds(base, KWIN), :], nt_dims, preferred_element_type=F32)
                    bias = band[...] + _range_bias(base, seq)
                    for hh in range(2):
                        rows = slice(hh * QBLK, (hh + 1) * QBLK)
                        sh = sc[rows, :] + bias
                        s_buf[slot, rows, :] = sh
                        m_buf[slot, rows, :] = jnp.broadcast_to(jnp.max(sh, axis=-1, keepdims=True), (QBLK, LANES))

                def outputs(b, slot):
                    base = _block_base(b)
                    sv = s_buf[slot]
                    mb = m_buf[slot]
                    p = jnp.exp(jnp.concatenate([sv[:, 0:LANES] - mb, sv[:, LANES:2 * LANES] - mb], axis=1))
                    pv = jnp.dot(p.astype(BF16), v_rm[pl.ds(base, KWIN), :], preferred_element_type=F32)
                    rows = pl.ds(base, QBLK)
                    acc_rm[rows, :] = jnp.where(low, pv[0:QBLK, 0:LANES], pv[QBLK:2 * QBLK, 0:LANES])
                    l_rm[rows, :] = jnp.where(low, pv[0:QBLK, LANES:2 * LANES], pv[QBLK:2 * QBLK, LANES:2 * LANES])
                    m_rm[rows, :] = jnp.where(low, mb[0:QBLK, :], mb[QBLK:2 * QBLK, :])

                scores(0, 0)

                def pair(i, carry):
                    b = 2 * i
                    outputs(b, 0)
                    scores(b + 1, 1)
                    outputs(b + 1, 1)
                    scores(b + 2, 0)
                    return carry

                lax.fori_loop(0, n_blk // 2 - 1, pair, 0)
                outputs(n_blk - 2, 0)
                scores(n_blk - 1, 1)
                outputs(n_blk - 1, 1)
                if d == 1:
                    src = (acc_rm, m_rm, l_rm)
                else:
                    for dst_, src_ in ((acc_p, acc_rm), (m_p, m_rm), (l_p, l_rm)):
                        _from_residue_major(dst_, src_, s, d)
                    src = (acc_p, m_p, l_p)

                def combine(rows):
                    a_g, m_g, l_g = src[0][rows, :], src[1][rows, :], src[2][rows, :]
                    if gi == 0:
                        m_new, l_new, a_new = m_g, l_g, a_g
                    else:
                        m_old = m_run[rows, :]
                        m_new = jnp.maximum(m_old, m_g)
                        w_old = jnp.exp(m_old - m_new)
                        w_g = jnp.exp(m_g - m_new)
                        l_new = l_run[rows, :] * w_old + l_g * w_g
                        a_new = acc_run[rows, :] * w_old + a_g * w_g
                    if gi == len(DILATIONS) - 1:
                        out = a_new / l_new
                        attn_ref[rows, :] = out
                        attn_b_ref[rows, :] = out.astype(BF16)
                        lse_ref[rows, :] = m_new + jnp.log(l_new)
                    else:
                        m_run[rows, :] = m_new
                        l_run[rows, :] = l_new
                        acc_run[rows, :] = a_new

                _row_chunks(s, combine)

    qk_spec = pl.BlockSpec((s, LANES), lambda hp, g: (0, g * n_pairs + hp))
    v_spec = pl.BlockSpec((s, LANES), lambda hp, g: (0, v_col0 + g * n_pairs + hp))
    o_spec = pl.BlockSpec((s, LANES), lambda hp, g: (0, hp))
    f32buf = pltpu.VMEM((s, LANES), F32)
    return _pallas(
        body, name=name, grid=(n_pairs, len(DILATIONS)), in_specs=[qk_spec, qk_spec, v_spec],
        out_specs=[o_spec, o_spec, o_spec],
        out_shape=[jax.ShapeDtypeStruct((s, n_pairs * LANES), F32)] * 2
        + [jax.ShapeDtypeStruct((s, n_pairs * LANES), BF16)],
        operands=[qn, kn, proj],
        scratch_shapes=[pltpu.VMEM((s, LANES), BF16), pltpu.VMEM((s + 2 * HALF_SPAN, LANES), BF16),
                        pltpu.VMEM((s + 2 * HALF_SPAN, 2 * LANES), BF16)] + [f32buf] * 9
        + [pltpu.VMEM((QBLK, KWIN), F32), pltpu.VMEM((2, 2 * QBLK, KWIN), F32),
           pltpu.VMEM((2, 2 * QBLK, LANES), F32)],
        rider=rider)


def _attn_bwd(qn, kn, proj, dattn, attn, lse, bd, name, rider=None):
    s = qn.shape[0]
    n_pairs = N_SLOT_HEADS * HEAD_DIM // LANES
    v_col0 = 2 * qn.shape[1] // LANES
    nt_dims = (((1,), (1,)), ((), ()))
    tn_dims = (((0,), (0,)), ((), ()))
    spad = s + 2 * HALF_SPAN

    def body(q_ref, k_ref, v_ref, do_ref, o_ref, lse_ref, bd_ref, dq_ref, dk_ref, dv_ref,
             q_rm, k_rm, v_rm, do_rm, lse0_rm, lse1_rm, dd0_rm, dd1_rm, dq_rm, dk_rm, dv_rm,
             lse0_p, lse1_p, dd0_p, dd1_p, band, p_buf, ds_buf):
        g = pl.program_id(1)
        zpad = jnp.zeros((HALF_SPAN, LANES), BF16)
        for buf in (k_rm, v_rm):
            buf[0:HALF_SPAN, :] = zpad
            buf[s + HALF_SPAN: spad, :] = zpad
        zf = jnp.zeros((HALF_SPAN, LANES), F32)
        for buf in (dk_rm, dv_rm):
            buf[0:HALF_SPAN, :] = zf
            buf[s + HALF_SPAN: spad, :] = zf
        band[...] = _band_bias()

        def clear(rows):
            z = jnp.zeros((rows.size, LANES), F32)
            dk_rm[pl.ds(rows.start + HALF_SPAN, rows.size), :] = z
            dv_rm[pl.ds(rows.start + HALF_SPAN, rows.size), :] = z

        _row_chunks(s, clear)

        def prepare(rows):
            lo = lax.broadcasted_iota(jnp.int32, (rows.size, LANES), 1) < HEAD_DIM
            dsum = _head_sums(do_ref[rows, :] * o_ref[rows, :], bd_ref[...])
            dswap = pltpu.roll(dsum, HEAD_DIM, axis=1)
            dd0_p[rows, :] = jnp.where(lo, dsum, dswap)
            dd1_p[rows, :] = jnp.where(lo, dswap, dsum)
            lv = lse_ref[rows, :]
            lswap = pltpu.roll(lv, HEAD_DIM, axis=1)
            lse0_p[rows, :] = jnp.where(lo, lv, lswap)
            lse1_p[rows, :] = jnp.where(lo, lswap, lv)

        @pl.when(g == 0)
        def _():
            _row_chunks(s, prepare)
        lane = lax.broadcasted_iota(jnp.int32, (QBLK, LANES), 1)
        low = lane < HEAD_DIM
        n_blk = s // QBLK

        def stacked(ref, rows):
            val = ref[rows, :]
            zero = jnp.zeros_like(val)
            return jnp.concatenate([jnp.where(low, val, zero), jnp.where(low, zero, val)], axis=0)

        for gi, d in enumerate(DILATIONS):
            @pl.when(g == gi)
            def _(d=d):
                seq = s // d
                _to_residue_major(q_rm, q_ref, s, d, cast=BF16)
                _to_residue_major(k_rm, k_ref, s, d, dst_off=HALF_SPAN, cast=BF16)
                _to_residue_major(v_rm, v_ref, s, d, dst_off=HALF_SPAN, cast=BF16)
                _to_residue_major(do_rm, do_ref, s, d, cast=BF16)
                for dst_, src_ in ((lse0_rm, lse0_p), (lse1_rm, lse1_p), (dd0_rm, dd0_p), (dd1_rm, dd1_p)):
                    _to_residue_major(dst_, src_, s, d)

                def scores(b, slot):
                    base = _block_base(b)
                    rows = pl.ds(base, QBLK)
                    win = pl.ds(base, KWIN)
                    sc = lax.dot_general(stacked(q_rm, rows), k_rm[win, :], nt_dims, preferred_element_type=F32)
                    dp = lax.dot_general(stacked(do_rm, rows), v_rm[win, :], nt_dims, preferred_element_type=F32)
                    bias = band[...] + _range_bias(base, seq)
                    for hh, (lse_r, dd_r) in enumerate(((lse0_rm, dd0_rm), (lse1_rm, dd1_rm))):
                        r = slice(hh * QBLK, (hh + 1) * QBLK)
                        lse_h = lse_r[rows, :]
                        dd_h = dd_r[rows, :]
                        sh = sc[r, :] + bias
                        p = jnp.exp(jnp.concatenate([sh[:, 0:LANES] - lse_h, sh[:, LANES:KWIN] - lse_h], axis=1))
                        dph = dp[r, :]
                        ds = p * jnp.concatenate([dph[:, 0:LANES] - dd_h, dph[:, LANES:KWIN] - dd_h], axis=1)
                        p_buf[slot, r, :] = p.astype(BF16)
                        ds_buf[slot, r, :] = ds.astype(BF16)

                def grads(b, slot):
                    base = _block_base(b)
                    rows = pl.ds(base, QBLK)
                    win = pl.ds(base, KWIN)
                    p = p_buf[slot]
                    ds = ds_buf[slot]
                    dq2 = jnp.dot(ds, k_rm[win, :], preferred_element_type=F32)
                    dq_rm[rows, :] = jnp.where(low, dq2[0:QBLK, :], dq2[QBLK:2 * QBLK, :])
                    dk_rm[win, :] += lax.dot_general(ds, stacked(q_rm, rows), tn_dims, preferred_element_type=F32)
                    dv_rm[win, :] += lax.dot_general(p, stacked(do_rm, rows), tn_dims, preferred_element_type=F32)

                scores(0, 0)

                def pair(i, carry):
                    b = 2 * i
                    grads(b, 0)
                    scores(b + 1, 1)
                    grads(b + 1, 1)
                    scores(b + 2, 0)
                    return carry

                lax.fori_loop(0, n_blk // 2 - 1, pair, 0)
                grads(n_blk - 2, 0)
                scores(n_blk - 1, 1)
                grads(n_blk - 1, 1)
                _from_residue_major(dq_ref, dq_rm, s, d)
                _from_residue_major(dk_ref, dk_rm, s, d, src_off=HALF_SPAN)
                _from_residue_major(dv_ref, dv_rm, s, d, src_off=HALF_SPAN)

    qk_spec = pl.BlockSpec((s, LANES), lambda hp, g: (0, g * n_pairs + hp))
    v_spec = pl.BlockSpec((s, LANES), lambda hp, g: (0, v_col0 + g * n_pairs + hp))
    o_spec = pl.BlockSpec((s, LANES), lambda hp, g: (0, hp))
    width = qn.shape[1]
    f32buf = pltpu.VMEM((s, LANES), F32)
    f32pad = pltpu.VMEM((spad, LANES), F32)
    return _pallas(
        body, name=name, grid=(n_pairs, len(DILATIONS)),
        in_specs=[qk_spec, qk_spec, v_spec, o_spec, o_spec, o_spec,
                  pl.BlockSpec((LANES, LANES), lambda hp, g: (0, 0))],
        out_specs=[qk_spec, qk_spec, qk_spec],
        out_shape=[jax.ShapeDtypeStruct((s, width), F32)] * 3,
        operands=[qn, kn, proj, dattn, attn, lse, bd],
        scratch_shapes=[pltpu.VMEM((s, LANES), BF16), pltpu.VMEM((spad, LANES), BF16),
                        pltpu.VMEM((spad, LANES), BF16), pltpu.VMEM((s, LANES), BF16),
                        f32buf, f32buf, f32buf, f32buf, f32buf, f32pad, f32pad,
                        f32buf, f32buf, f32buf, f32buf, pltpu.VMEM((QBLK, KWIN), F32),
                        pltpu.VMEM((2, 2 * QBLK, KWIN), BF16), pltpu.VMEM((2, 2 * QBLK, KWIN), BF16)],
        rider=rider)


CONV_PAD = 16


def _conv_fwd(proj, conv_w, conv_b, col0, name, rider=None):
    s = proj.shape[0]
    ch = conv_w.shape[1]
    nblk = ch // LANES
    a0 = col0 // LANES
    tr = 256
    shift = CONV_PAD - (CONV_WIDTH - 1) // 2

    def body(a_ref, b_ref, w_ref, bias_ref, u0_ref, uc_ref, pad):
        z = jnp.zeros((CONV_PAD, LANES), F32)
        pad[0:CONV_PAD, :] = z
        pad[s + CONV_PAD: s + 2 * CONV_PAD, :] = z

        def glu(rows):
            u0 = a_ref[rows, :] * jax.nn.sigmoid(b_ref[rows, :])
            u0_ref[rows, :] = u0
            pad[pl.ds(rows.start + CONV_PAD, rows.size), :] = u0

        _row_chunks(s, glu)
        for t in range(0, s, tr):
            acc = jnp.broadcast_to(bias_ref[...], (tr, LANES))
            for k in range(CONV_WIDTH):
                acc = acc + w_ref[k:k + 1, :] * pad[t + k + shift: t + k + shift + tr, :]
            uc_ref[t:t + tr, :] = acc

    return _pallas(
        body, name=name, grid=(nblk,),
        in_specs=[pl.BlockSpec((s, LANES), lambda c: (0, a0 + c)),
                  pl.BlockSpec((s, LANES), lambda c: (0, a0 + nblk + c)),
                  pl.BlockSpec((CONV_WIDTH, LANES), lambda c: (0, c)),
                  pl.BlockSpec((1, LANES), lambda c: (0, c))],
        out_specs=[pl.BlockSpec((s, LANES), lambda c: (0, c))] * 2,
        out_shape=[jax.ShapeDtypeStruct((s, ch), F32)] * 2, operands=[proj, proj, conv_w, conv_b],
        scratch_shapes=[pltpu.VMEM((s + 2 * CONV_PAD, LANES), F32)], rider=rider)


def _ln_silu_fwd(uc, ln_w, ln_b, name):
    s, ch = uc.shape
    tm = 256

    def body(u_ref, w_ref, b_ref, o_ref):
        u = u_ref[...]
        mu = jnp.mean(u, axis=-1, keepdims=True)
        xc = u - mu
        rstd = lax.rsqrt(jnp.mean(xc * xc, axis=-1, keepdims=True) + EPS)
        z = xc * rstd * w_ref[...] + b_ref[...]
        o_ref[...] = (z * jax.nn.sigmoid(z)).astype(BF16)

    row = pl.BlockSpec((tm, ch), lambda i: (i, 0))
    vec = pl.BlockSpec((1, ch), lambda i: (0, 0))
    return pl.pallas_call(
        body, name=name, grid=(s // tm,), in_specs=[row, vec, vec], out_specs=row,
        out_shape=jax.ShapeDtypeStruct((s, ch), BF16), compiler_params=_params(),
    )(uc, ln_w, ln_b)


def _ln_silu_bwd(du3, uc, ln_w, ln_b, name):
    s, ch = uc.shape
    tm = 256

    def body(d_ref, u_ref, w_ref, b_ref, du_ref, dw_ref, db_ref):
        u = u_ref[...]
        mu = jnp.mean(u, axis=-1, keepdims=True)
        xc = u - mu
        rstd = lax.rsqrt(jnp.mean(xc * xc, axis=-1, keepdims=True) + EPS)
        xhat = xc * rstd
        z = xhat * w_ref[...] + b_ref[...]
        sg = jax.nn.sigmoid(z)
        dz = d_ref[...] * (sg * (1.0 + z * (1.0 - sg)))
        dxh = dz * w_ref[...]
        du_ref[...] = rstd * (dxh - jnp.mean(dxh, axis=-1, keepdims=True)
                              - xhat * jnp.mean(dxh * xhat, axis=-1, keepdims=True))
        pw = jnp.sum(dz * xhat, axis=0, keepdims=True)
        pb = jnp.sum(dz, axis=0, keepdims=True)
        first = pl.program_id(0) == 0

        @pl.when(first)
        def _():
            dw_ref[...] = pw
            db_ref[...] = pb

        @pl.when(jnp.logical_not(first))
        def _():
            dw_ref[...] += pw
            db_ref[...] += pb

    row = pl.BlockSpec((tm, ch), lambda i: (i, 0))
    vec = pl.BlockSpec((1, ch), lambda i: (0, 0))
    return pl.pallas_call(
        body, name=name, grid=(s // tm,), in_specs=[row, row, vec, vec], out_specs=[row, vec, vec],
        out_shape=[jax.ShapeDtypeStruct((s, ch), F32), jax.ShapeDtypeStruct((1, ch), F32),
                   jax.ShapeDtypeStruct((1, ch), F32)],
        compiler_params=_params(),
    )(du3, uc, ln_w, ln_b)


def _conv_bwd(duc, u0, proj, conv_w, col0, name, rider=None):
    s = proj.shape[0]
    ch = conv_w.shape[1]
    nblk = ch // LANES
    a0 = col0 // LANES
    tr = 256
    half = (CONV_WIDTH - 1) // 2
    shift = CONV_PAD - half

    def body(duc_ref, u0_ref, a_ref, b_ref, w_ref, da_ref, db_ref, dw_ref, dbias_ref, pad_d, pad_u):
        z = jnp.zeros((CONV_PAD, LANES), F32)
        for buf in (pad_d, pad_u):
            buf[0:CONV_PAD, :] = z
            buf[s + CONV_PAD: s + 2 * CONV_PAD, :] = z

        def fill(rows):
            dst = pl.ds(rows.start + CONV_PAD, rows.size)
            pad_d[dst, :] = duc_ref[rows, :]
            pad_u[dst, :] = u0_ref[rows, :]

        _row_chunks(s, fill)
        dw_acc = [jnp.zeros((8, LANES), F32) for _ in range(CONV_WIDTH)]
        dbias_acc = jnp.zeros((8, LANES), F32)
        for t in range(0, s, tr):
            d_t = duc_ref[t:t + tr, :]
            dbias_acc = dbias_acc + jnp.sum(d_t.reshape(tr // 8, 8, LANES), axis=0)
            du0 = jnp.zeros((tr, LANES), F32)
            for k in range(CONV_WIDTH):
                du0 = du0 + w_ref[k:k + 1, :] * pad_d[t - k + half + CONV_PAD: t - k + half + CONV_PAD + tr, :]
                prod = d_t * pad_u[t + k + shift: t + k + shift + tr, :]
                dw_acc[k] = dw_acc[k] + jnp.sum(prod.reshape(tr // 8, 8, LANES), axis=0)
            av = a_ref[t:t + tr, :]
            sg = jax.nn.sigmoid(b_ref[t:t + tr, :])
            da_ref[t:t + tr, :] = (du0 * sg).astype(BF16)
            db_ref[t:t + tr, :] = (du0 * av * sg * (1.0 - sg)).astype(BF16)
        for k in range(CONV_WIDTH):
            dw_ref[k:k + 1, :] = jnp.sum(dw_acc[k], axis=0, keepdims=True)
        dbias_ref[...] = jnp.sum(dbias_acc, axis=0, keepdims=True)

    col = lambda off: pl.BlockSpec((s, LANES), lambda c: (0, off + c))
    return _pallas(
        body, name=name, grid=(nblk,),
        in_specs=[col(0), col(0), col(a0), col(a0 + nblk),
                  pl.BlockSpec((CONV_WIDTH, LANES), lambda c: (0, c))],
        out_specs=[col(0), col(0), pl.BlockSpec((CONV_WIDTH, LANES), lambda c: (0, c)),
                   pl.BlockSpec((1, LANES), lambda c: (0, c))],
        out_shape=[jax.ShapeDtypeStruct((s, ch), BF16)] * 2
        + [jax.ShapeDtypeStruct((CONV_WIDTH, ch), F32), jax.ShapeDtypeStruct((1, ch), F32)],
        operands=[duc, u0, proj, proj, conv_w],
        scratch_shapes=[pltpu.VMEM((s + 2 * CONV_PAD, LANES), F32)] * 2, rider=rider)


GATE_BLK = 512


def _gate_fwd(proj, bg, y_a, y_b, col0, name):
    s, d = y_a.shape
    tm = 256
    g0 = col0 // GATE_BLK
    nb = d // GATE_BLK

    def body(ga_ref, gb_ref, ba_ref, bb_ref, ya_ref, yb_ref, o_ref):
        g_a = jax.nn.sigmoid(ga_ref[...] + ba_ref[...])
        g_b = jax.nn.sigmoid(gb_ref[...] + bb_ref[...])
        o_ref[...] = (g_a * ya_ref[...] + g_b * yb_ref[...]).astype(BF16)

    act = pl.BlockSpec((tm, GATE_BLK), lambda i, j: (i, j))
    return pl.pallas_call(
        body, name=name, grid=(s // tm, nb),
        in_specs=[pl.BlockSpec((tm, GATE_BLK), lambda i, j: (i, g0 + j)),
                  pl.BlockSpec((tm, GATE_BLK), lambda i, j: (i, g0 + nb + j)),
                  pl.BlockSpec((None, 1, GATE_BLK), lambda i, j: (0, 0, j)),
                  pl.BlockSpec((None, 1, GATE_BLK), lambda i, j: (1, 0, j)), act, act],
        out_specs=act, out_shape=jax.ShapeDtypeStruct((s, d), BF16), compiler_params=_params(),
    )(proj, proj, bg, bg, y_a, y_b)


def _gate_bwd(d_mixed, proj, bg, y_a, y_b, col0, name, rider=None):
    s, d = y_a.shape
    tm = 256
    half = d // 2
    assert col0 % half == 0
    c0 = col0 // half

    def body(dm_ref, a0_ref, a1_ref, b0_ref, b1_ref, bias_ref, ya_ref, yb_ref, dgl_ref, dya_ref, dyb_ref, db_ref):
        dm = dm_ref[...]
        parts = []
        for br, (lo_ref, hi_ref, y_ref, dy_ref) in enumerate(((a0_ref, a1_ref, ya_ref, dya_ref),
                                                              (b0_ref, b1_ref, yb_ref, dyb_ref))):
            logits = jnp.concatenate([lo_ref[...], hi_ref[...]], axis=1)
            gate = jax.nn.sigmoid(logits + bias_ref[br])
            dy_ref[...] = (dm * gate).astype(BF16)
            dgl = dm * y_ref[...] * gate * (1.0 - gate)
            dgl_ref[:, br * d:(br + 1) * d] = dgl.astype(BF16)
            parts.append(jnp.sum(dgl, axis=0, keepdims=True))
        part = jnp.concatenate(parts, axis=0)
        first = pl.program_id(0) == 0

        @pl.when(first)
        def _():
            db_ref[...] = part

        @pl.when(jnp.logical_not(first))
        def _():
            db_ref[...] += part

    row = pl.BlockSpec((tm, d), lambda i: (i, 0))
    logit_blk = lambda k: pl.BlockSpec((tm, half), functools.partial(lambda i, k: (i, c0 + k), k=k))
    return _pallas(
        body, name=name, grid=(s // tm,),
        in_specs=[row, logit_blk(0), logit_blk(1), logit_blk(2), logit_blk(3),
                  pl.BlockSpec((2, 1, d), lambda i: (0, 0, 0)), row, row],
        out_specs=[pl.BlockSpec((tm, 2 * d), lambda i: (i, 0)), row, row, pl.BlockSpec((2, d), lambda i: (0, 0))],
        out_shape=[jax.ShapeDtypeStruct((s, 2 * d), BF16), jax.ShapeDtypeStruct((s, d), BF16),
                   jax.ShapeDtypeStruct((s, d), BF16), jax.ShapeDtypeStruct((2, d), F32)],
        operands=[d_mixed, proj, proj, proj, proj, bg, y_a, y_b], rider=rider)


def _ffn_in_swiglu(h2, w_blocked, name):
    s, k = h2.shape
    nblk, _, tn = w_blocked.shape
    ff = nblk // 2 * tn
    tm = 512

    def body(a_ref, wg_ref, wu_ref, g_ref, u_ref, act_ref):
        a = a_ref[...]
        gt = jnp.dot(a, wg_ref[...], preferred_element_type=F32)
        up = jnp.dot(a, wu_ref[...], preferred_element_type=F32)
        g_ref[...] = gt
        u_ref[...] = up
        act_ref[...] = (gt * jax.nn.sigmoid(gt) * up).astype(BF16)

    out = pl.BlockSpec((tm, tn), lambda i, j: (i, j))
    return pl.pallas_call(
        body, name=name, grid=(s // tm, nblk // 2),
        in_specs=[pl.BlockSpec((tm, k), lambda i, j: (i, 0)),
                  pl.BlockSpec((None, k, tn), lambda i, j: (j, 0, 0)),
                  pl.BlockSpec((None, k, tn), lambda i, j: (nblk // 2 + j, 0, 0))],
        out_specs=[out, out, out],
        out_shape=[jax.ShapeDtypeStruct((s, ff), F32), jax.ShapeDtypeStruct((s, ff), F32),
                   jax.ShapeDtypeStruct((s, ff), BF16)],
        compiler_params=_params(),
    )(h2, w_blocked, w_blocked)


def _swiglu_bwd(gate, up, d_act, name, rider=None):
    s, ff = gate.shape
    tm = 256

    def body(g_ref, u_ref, d_ref, o_ref):
        gt = g_ref[...]
        sg = jax.nn.sigmoid(gt)
        dv = d_ref[...]
        o_ref[:, 0:ff] = (dv * u_ref[...] * (sg * (1.0 + gt * (1.0 - sg)))).astype(BF16)
        o_ref[:, ff:2 * ff] = (dv * gt * sg).astype(BF16)

    row = pl.BlockSpec((tm, ff), lambda i: (i, 0))
    return _pallas(
        body, name=name, grid=(s // tm,), in_specs=[row, row, row],
        out_specs=pl.BlockSpec((tm, 2 * ff), lambda i: (i, 0)),
        out_shape=jax.ShapeDtypeStruct((s, 2 * ff), BF16), operands=[gate, up, d_act], rider=rider)


def _out_proj_rmsnorm(mixed, w_out, x, norm_w, name):
    s, k = mixed.shape
    d = w_out.shape[1]
    tm = 512

    def body(a_ref, w_ref, x_ref, nw_ref, x1_ref, h2_ref):
        x1 = x_ref[...] + jnp.dot(a_ref[...], w_ref[...], preferred_element_type=F32)
        x1_ref[...] = x1
        rstd = lax.rsqrt(jnp.mean(x1 * x1, axis=-1, keepdims=True) + EPS)
        h2_ref[...] = (x1 * rstd * nw_ref[...]).astype(BF16)

    row = pl.BlockSpec((tm, d), lambda i: (i, 0))
    return pl.pallas_call(
        body, name=name, grid=(s // tm,),
        in_specs=[pl.BlockSpec((tm, k), lambda i: (i, 0)), pl.BlockSpec((k, d), lambda i: (0, 0)), row,
                  pl.BlockSpec((1, d), lambda i: (0, 0))],
        out_specs=[row, row],
        out_shape=[jax.ShapeDtypeStruct((s, d), F32), jax.ShapeDtypeStruct((s, d), BF16)],
        compiler_params=_params(),
    )(mixed, w_out, x, norm_w)


def _ffn_out_loss(act, w_ffn_out, x1, target, name):
    s, k = act.shape
    d = w_ffn_out.shape[1]
    tm = 512

    def body(a_ref, w_ref, x1_ref, t_ref, dy_ref, dyb_ref, loss_ref, acc):
        y = x1_ref[...] + jnp.dot(a_ref[...], w_ref[...], preferred_element_type=F32)
        diff = y - t_ref[...]
        dy = diff * (1.0 / d)
        dy_ref[...] = dy
        dyb_ref[...] = dy.astype(BF16)
        part = jnp.sum((diff * diff).reshape(tm // 8, 8, d), axis=0)
        i = pl.program_id(0)

        @pl.when(i == 0)
        def _():
            acc[...] = part

        @pl.when(i > 0)
        def _():
            acc[...] += part

        @pl.when(i == pl.num_programs(0) - 1)
        def _():
            loss_ref[...] = (0.5 / d) * jnp.sum(jnp.sum(acc[...], axis=1, keepdims=True), axis=0, keepdims=True)

    row = pl.BlockSpec((tm, d), lambda i: (i, 0))
    return pl.pallas_call(
        body, name=name, grid=(s // tm,),
        in_specs=[pl.BlockSpec((tm, k), lambda i: (i, 0)), pl.BlockSpec((k, d), lambda i: (0, 0)), row, row],
        out_specs=[row, row, pl.BlockSpec((1, 1), lambda i: (0, 0))],
        out_shape=[jax.ShapeDtypeStruct((s, d), F32), jax.ShapeDtypeStruct((s, d), BF16),
                   jax.ShapeDtypeStruct((1, 1), F32)],
        scratch_shapes=[pltpu.VMEM((8, d), F32)], compiler_params=_params(),
    )(act, w_ffn_out, x1, target)


LATE_GATHER = ("w_o_attn", "w_pw_conv", "w_out", "w_ffn_in", "w_ffn_out")
EARLY_REDUCE = LATE_GATHER


def _blocks_by_half(g):
    if g.ndim == 2:
        g = g.reshape(N_CHIPS, g.shape[0] // N_CHIPS, g.shape[1])
    return g.reshape(N_CHIPS, 2, g.shape[1] // 2, g.shape[2])


def _forward_backward(x, pos_col, target, wts, late_bufs, pos_arr):
    wts = dict(wts)
    consts = _rope_consts()
    bd = consts[3]
    qw2 = jnp.tile(wts["q_norm_w"], (1, LANES // HEAD_DIM))
    kw2 = jnp.tile(wts["k_norm_w"], (1, LANES // HEAD_DIM))
    qkv_w = 3 * N_SLOT_HEADS * HEAD_DIM
    conv_col0 = 3 * qkv_w
    ch = wts["conv_w"].shape[1]
    gate_col0 = conv_col0 + 2 * ch

    h = _rmsnorm_fwd(x, wts["norm1_w"], "rms1_fwd")
    late = dict(zip(LATE_GATHER, late_bufs))
    quarter = late["w_ffn_in"].shape[2] // 4
    proj, (late["w_o_attn"], late["w_pw_conv"], late["w_out"], late["w_ffn_in"]) = _matmul(
        h, wts["w_in"], mode="nn", tm=512, tn=1920, tk=1024, out_dtype=F32, name="mm_proj", b_blocked=True,
        rider=_gather_ici_rider([late["w_o_attn"], late["w_pw_conv"], late["w_out"], late["w_ffn_in"]], [],
                                row_ranges=[None, None, None, (0, quarter)]))
    (qn, kn), (late["w_ffn_in"],) = _qk_fwd(
        proj, pos_col, qw2, kw2, consts, "qk_fwd",
        rider=_gather_ici_rider([late["w_ffn_in"]], [], row_ranges=[(quarter, 2 * quarter)]))
    (attn, lse, attn_b), (late["w_ffn_in"], late["w_ffn_out"]) = _attn_fwd(
        qn, kn, proj, "attn_fwd",
        rider=_gather_ici_rider([late["w_ffn_in"], late["w_ffn_out"]], [],
                                row_ranges=[(3 * quarter, quarter), None]))
    (u0, uc), late_bufs = _conv_fwd(proj, wts["conv_w"], wts["conv_b"], conv_col0, "conv_fwd",
                                    rider=_gather_forward_rider([late[n] for n in LATE_GATHER]))
    for n, buf in zip(LATE_GATHER, late_bufs):
        full = buf.reshape(N_CHIPS, -1, buf.shape[3])
        wts[n] = full.reshape(-1, full.shape[2]) if n in ROW_SHARDED else full
    y_a = _matmul(attn_b, wts["w_o_attn"], mode="nn", tm=1024, tn=256, tk=512, out_dtype=F32, name="mm_ya",
                  b_blocked=True)
    u3 = _ln_silu_fwd(uc, wts["conv_ln_w"], wts["conv_ln_b"], "ln_fwd")
    y_b = _matmul(u3, wts["w_pw_conv"], mode="nn", tm=1024, tn=256, tk=512, out_dtype=F32, name="mm_yb",
                  b_blocked=True)
    mixed = _gate_fwd(proj, wts["b_gate"], y_a, y_b, gate_col0, "gate_fwd")
    x1, h2 = _out_proj_rmsnorm(mixed, wts["w_out"], x, wts["norm2_w"], "mm_x1_rms2")
    gate, up, act = _ffn_in_swiglu(h2, wts["w_ffn_in"], "mm_gu_swiglu")
    dy, dy_b16, loss = _ffn_out_loss(act, wts["w_ffn_out"], x1, target, "mm_x2_loss")

    g = {}
    by_chip = {}

    def pair_add(n, blocks, received):
        return _add_own_half(blocks, received, pos_arr, f"grads_pair_add_{n}")

    d_act = _matmul(dy_b16, wts["w_ffn_out"], mode="nt", tm=512, tn=1408, tk=1024, out_dtype=F32, name="mm_dact")
    g_ffn_out = _blocks_by_half(
        _matmul(act, dy_b16, mode="tn", tm=1408, tn=1024, tk=2048, out_dtype=F32, name="mm_dwffnout"))
    dgu, (received,) = _swiglu_bwd(gate, up, d_act, "swiglu_bwd",
                                   rider=_pair_exchange_rider([g_ffn_out], halved=True))
    to_send, own = pair_add("w_ffn_out", g_ffn_out, received)
    dh2, (by_chip["w_ffn_out"],) = _matmul(
        dgu, wts["w_ffn_in"], mode="nt", tm=1024, tn=1024, tk=1408, out_dtype=F32, name="mm_dh2", b_blocked=True,
        rider=_chip_exchange_rider([to_send], [own]))
    g_ffn_in = _blocks_by_half(_matmul(h2, dgu, mode="tn", tm=512, tn=1408, tk=2048, out_dtype=F32,
                                       name="mm_dwffnin", out_blocked=N_CHIPS))
    dx1, dx1_b16, g["norm2_w"] = _rmsnorm_bwd(dh2, x1, wts["norm2_w"], dy, "rms2_bwd")
    d_mixed = _matmul(dx1_b16, wts["w_out"], mode="nt", tm=512, tn=1024, tk=1024, out_dtype=F32, name="mm_dmixed")
    g["w_out"] = _matmul(mixed, dx1_b16, mode="tn", tm=512, tn=1024, tk=2048, out_dtype=F32, name="mm_dwout")
    (dgl, dy_a, dy_b, g["b_gate"]), (received,) = _gate_bwd(
        d_mixed, proj, wts["b_gate"], y_a, y_b, gate_col0, "gate_bwd",
        rider=_pair_exchange_rider([g_ffn_in], halved=True))
    ffn_in_to_send, ffn_in_own = pair_add("w_ffn_in", g_ffn_in, received)
    dattn = _matmul(dy_a, wts["w_o_attn"], mode="nt", tm=1024, tn=512, tk=256, out_dtype=F32, name="mm_dattn",
                    b_blocked=True)
    g["w_o_attn"] = _matmul(attn_b, dy_a, mode="tn", tm=512, tn=256, tk=2048, out_dtype=F32, name="mm_dwo",
                            out_blocked=N_CHIPS)
    du3 = _matmul(dy_b, wts["w_pw_conv"], mode="nt", tm=1024, tn=512, tk=256, out_dtype=F32, name="mm_du3",
                  b_blocked=True)
    g["w_pw_conv"] = _matmul(u3, dy_b, mode="tn", tm=512, tn=256, tk=2048, out_dtype=F32, name="mm_dwpw",
                             out_blocked=N_CHIPS)
    duc, g["conv_ln_w"], g["conv_ln_b"] = _ln_silu_bwd(du3, uc, wts["conv_ln_w"], wts["conv_ln_b"], "ln_bwd")

    small3 = ("w_out", "w_o_attn", "w_pw_conv")
    g_small3 = [_blocks_by_half(g.pop(n)) for n in small3]
    (da, db, g["conv_w"], g["conv_b"]), received = _conv_bwd(
        duc, u0, proj, wts["conv_w"], conv_col0, "conv_bwd", rider=_pair_exchange_rider(g_small3, halved=True))
    sums3 = [pair_add(n, gb, rv) for n, gb, rv in zip(small3, g_small3, received)]
    (dqn, dkn, dv), (by_chip["w_ffn_in"],) = _attn_bwd(
        qn, kn, proj, dattn, attn, lse, bd, "attn_bwd",
        rider=_chip_exchange_rider([ffn_in_to_send], [ffn_in_own]))
    (dproj, dqw, dkw), exchanged3 = _qk_bwd(
        dqn, dkn, dv, da, db, dgl, proj, pos_col, qw2, kw2, consts, "qk_bwd",
        rider=_chip_exchange_rider([s[0] for s in sums3], [s[1] for s in sums3]))
    by_chip.update(zip(small3, exchanged3))
    halves = [_sum_chips(by_chip[n], pos_arr, f"grads_chip_sum_{n}") for n in EARLY_REDUCE]
    g["q_norm_w"] = dqw[:, :HEAD_DIM]
    g["k_norm_w"] = dkw[:, :HEAD_DIM]

    c = pos_arr[0]
    rh = h.shape[1] // 2
    h_sibling = lax.dynamic_slice_in_dim(h, (1 - c) * rh, rh, axis=1)
    h_own = lax.dynamic_slice_in_dim(h, c * rh, rh, axis=1)
    g_sibling, shards = _matmul(h_sibling, dproj, mode="tn", tm=rh, tn=1920, tk=2048, out_dtype=F32,
                                name="mm_dwin_sibling", out_blocked=N_CHIPS, rider=_pair_gather_rider(halves))
    reduced = dict(zip(EARLY_REDUCE, shards))
    g_own, from_sibling = _matmul(h_own, dproj, mode="tn", tm=rh, tn=1920, tk=2048, out_dtype=F32,
                                  name="mm_dwin_own", out_blocked=N_CHIPS,
                                  rider=_pair_exchange_rider([g_sibling], halved=False))
    to_send, own = _add_own_half(g_own, from_sibling[0], pos_arr, "grads_pair_add_w_in")
    *in_flight, token = _chip_exchange_start(to_send, own, "grads_w_in_exchange_start")
    in_flight = tuple(in_flight) + (token,)
    dh = _matmul(dproj, wts["w_in"], mode="nt", tm=1024, tn=1024, tk=1920, out_dtype=F32, name="mm_dh",
                 b_blocked=True, after=[token])
    grad_x, _, g["norm1_w"] = _rmsnorm_bwd(dh, x, wts["norm1_w"], dx1, "rms1_bwd")
    return loss, grad_x, g, reduced, in_flight


def _mesh_pos():
    return lax.axis_index("x"), lax.axis_index("y"), lax.axis_index("c")


def _other_chips(x, y):
    return [(1 - x, y), (x, 1 - y), (1 - x, 1 - y)]


def _cast_into_slot(shard, chip_arr, dtype, name):
    r, c = shard.shape
    tr = r // 2 if r % 32 == 0 else r

    def body(chip_ref, s_ref, o_ref):
        del chip_ref
        o_ref[...] = s_ref[...].astype(dtype)

    return pl.pallas_call(
        body, name=name,
        grid_spec=pltpu.PrefetchScalarGridSpec(
            num_scalar_prefetch=1, grid=(r // tr,),
            in_specs=[pl.BlockSpec((tr, c), lambda i, chip_ref: (i, 0))],
            out_specs=pl.BlockSpec((None, tr, c), lambda i, chip_ref: (chip_ref[0], i, 0))),
        out_shape=jax.ShapeDtypeStruct((N_CHIPS, r, c), dtype), compiler_params=_params(),
    )(chip_arr, shard)


def _allgather_inplace(big, small, name):
    nb, ns = len(big), len(small)
    n = nb + ns

    def body(*refs):
        bufs = refs[n:2 * n]
        send_sems, recv_sems, fsend_sems, frecv_sems = refs[2 * n:]
        x, y, c = _mesh_pos()
        me = 2 * x + y
        chips = _other_chips(x, y)

        def part(a, slot, half):
            return bufs[a].at[slot, half] if a < nb else bufs[a].at[slot]

        sends = []
        for a in range(n):
            for k, (px, py) in enumerate(chips):
                cp = pltpu.make_async_remote_copy(
                    src_ref=part(a, me, c), dst_ref=part(a, me, c), send_sem=send_sems.at[a, k],
                    recv_sem=recv_sems.at[a, k], device_id=(px, py, c), device_id_type=MESH)
                cp.start()
                sends.append(cp)
        for a in range(n):
            for k, (px, py) in enumerate(chips):
                slot = 2 * px + py
                pltpu.make_async_remote_copy(
                    src_ref=part(a, slot, c), dst_ref=part(a, slot, c), send_sem=send_sems.at[a, k],
                    recv_sem=recv_sems.at[a, k], device_id=(px, py, c), device_id_type=MESH).wait_recv()
                if a < nb:
                    fwd = pltpu.make_async_remote_copy(
                        src_ref=part(a, slot, c), dst_ref=part(a, slot, c), send_sem=fsend_sems.at[a, k],
                        recv_sem=frecv_sems.at[a, k], device_id=(x, y, 1 - c), device_id_type=MESH)
                    fwd.start()
                    sends.append(fwd)
        for a in range(nb):
            for k, (px, py) in enumerate(chips):
                slot = 2 * px + py
                pltpu.make_async_remote_copy(
                    src_ref=part(a, slot, 1 - c), dst_ref=part(a, slot, 1 - c), send_sem=fsend_sems.at[a, k],
                    recv_sem=frecv_sems.at[a, k], device_id=(x, y, 1 - c), device_id_type=MESH).wait_recv()
        for cp in sends:
            cp.wait_send()

    ops = list(big) + list(small)
    return pl.pallas_call(
        body, name=name, in_specs=[ANY] * n, out_specs=[ANY] * n,
        out_shape=[jax.ShapeDtypeStruct(o.shape, o.dtype) for o in ops],
        input_output_aliases={i: i for i in range(n)},
        scratch_shapes=[pltpu.SemaphoreType.DMA((n, 3)), pltpu.SemaphoreType.DMA((n, 3)),
                        pltpu.SemaphoreType.DMA((nb, 3)), pltpu.SemaphoreType.DMA((nb, 3))],
    )(*ops)


def _comm_call(rider, name):
    def body():
        pass

    return _pallas(body, name=name, grid=(1,), in_specs=[], out_specs=[], out_shape=[], operands=[],
                   rider=rider)[1]


def _gather_ici_rider(big, small, row_ranges=None):
    nb = len(big)
    n = nb + len(small)
    row_ranges = row_ranges or [None] * nb

    def copies(bufs, sems):
        x, y, c = _mesh_pos()
        me = 2 * x + y

        def part(a, slot):
            if a >= nb:
                return bufs[a].at[slot]
            if row_ranges[a] is None:
                return bufs[a].at[slot, c]
            return bufs[a].at[slot, c, pl.ds(*row_ranges[a])]
        out = []
        for a in range(n):
            for k, (px, py) in enumerate(_other_chips(x, y)):
                send = functools.partial(
                    pltpu.make_async_remote_copy,
                    src_ref=part(a, me), dst_ref=part(a, me), send_sem=sems[0].at[a, k],
                    recv_sem=sems[1].at[a, k], device_id=(px, py, c), device_id_type=MESH)
                recv = functools.partial(
                    pltpu.make_async_remote_copy,
                    src_ref=part(a, 2 * px + py), dst_ref=part(a, 2 * px + py), send_sem=sems[0].at[a, k],
                    recv_sem=sems[1].at[a, k], device_id=(px, py, c), device_id_type=MESH)
                out.append((send, recv))
        return out

    def start(r_in, r_out, sems):
        for send, _ in copies(r_out, sems):
            send().start()

    def wait(r_in, r_out, sems):
        cps = copies(r_out, sems)
        for _, recv in cps:
            recv().wait_recv()
        for send, _ in cps:
            send().wait_send()

    ops = list(big) + list(small)
    return _Rider(ops, [jax.ShapeDtypeStruct(o.shape, o.dtype) for o in ops], {i: i for i in range(n)},
                  [pltpu.SemaphoreType.DMA((n, 3)), pltpu.SemaphoreType.DMA((n, 3))], start, wait)


def _gather_forward_rider(big):
    n = len(big)

    def copies(bufs, sems):
        x, y, c = _mesh_pos()
        out = []
        for a in range(n):
            for k, (px, py) in enumerate(_other_chips(x, y)):
                slot = 2 * px + py
                send = functools.partial(
                    pltpu.make_async_remote_copy,
                    src_ref=bufs[a].at[slot, c], dst_ref=bufs[a].at[slot, c], send_sem=sems[0].at[a, k],
                    recv_sem=sems[1].at[a, k], device_id=(x, y, 1 - c), device_id_type=MESH)
                recv = functools.partial(
                    pltpu.make_async_remote_copy,
                    src_ref=bufs[a].at[slot, 1 - c], dst_ref=bufs[a].at[slot, 1 - c], send_sem=sems[0].at[a, k],
                    recv_sem=sems[1].at[a, k], device_id=(x, y, 1 - c), device_id_type=MESH)
                out.append((send, recv))
        return out

    def start(r_in, r_out, sems):
        for send, _ in copies(r_out, sems):
            send().start()

    def wait(r_in, r_out, sems):
        cps = copies(r_out, sems)
        for _, recv in cps:
            recv().wait_recv()
        for send, _ in cps:
            send().wait_send()

    return _Rider(big, [jax.ShapeDtypeStruct(o.shape, o.dtype) for o in big], {i: i for i in range(n)},
                  [pltpu.SemaphoreType.DMA((n, 3)), pltpu.SemaphoreType.DMA((n, 3))], start, wait)


def _pair_exchange_rider(gs, halved):
    n = len(gs)

    def copies(r_in, r_out, sems):
        x, y, c = _mesh_pos()
        return [pltpu.make_async_remote_copy(
            src_ref=r_in[a].at[:, 1 - c] if halved else r_in[a], dst_ref=r_out[a], send_sem=sems[0].at[a],
            recv_sem=sems[1].at[a], device_id=(x, y, 1 - c), device_id_type=MESH) for a in range(n)]

    def start(r_in, r_out, sems):
        for cp in copies(r_in, r_out, sems):
            cp.start()

    def wait(r_in, r_out, sems):
        for cp in copies(r_in, r_out, sems):
            cp.wait()

    return _Rider(gs, [jax.ShapeDtypeStruct((g.shape[0],) + g.shape[-2:], g.dtype) for g in gs], {},
                  [pltpu.SemaphoreType.DMA((n,)), pltpu.SemaphoreType.DMA((n,))], start, wait)


def _chip_exchange_rider(to_send, by_chip, row_range=None):
    n = len(to_send)

    def copies(r_in, r_out, sems):
        x, y, c = _mesh_pos()
        me = 2 * x + y
        rows = (lambda ref: ref) if row_range is None else (lambda ref: ref.at[pl.ds(*row_range)])
        out = []
        for a in range(n):
            for k, (px, py) in enumerate(_other_chips(x, y)):
                send = functools.partial(
                    pltpu.make_async_remote_copy,
                    src_ref=rows(r_in[a].at[2 * px + py]), dst_ref=rows(r_out[a].at[me]),
                    send_sem=sems[0].at[a, k], recv_sem=sems[1].at[a, k], device_id=(px, py, c),
                    device_id_type=MESH)
                recv = functools.partial(
                    pltpu.make_async_remote_copy,
                    src_ref=rows(r_in[a].at[me]), dst_ref=rows(r_out[a].at[2 * px + py]),
                    send_sem=sems[0].at[a, k], recv_sem=sems[1].at[a, k], device_id=(px, py, c),
                    device_id_type=MESH)
                out.append((send, recv))
        return out

    def start(r_in, r_out, sems):
        for send, _ in copies(r_in, r_out, sems):
            send().start()

    def wait(r_in, r_out, sems):
        cps = copies(r_in, r_out, sems)
        for _, recv in cps:
            recv().wait_recv()
        for send, _ in cps:
            send().wait_send()

    return _Rider(list(to_send) + list(by_chip), [jax.ShapeDtypeStruct(b.shape, b.dtype) for b in by_chip],
                  {n + i: i for i in range(n)},
                  [pltpu.SemaphoreType.DMA((n, 3)), pltpu.SemaphoreType.DMA((n, 3))], start, wait)


HBM = pl.BlockSpec(memory_space=pltpu.HBM)
SEM = pl.BlockSpec(memory_space=pltpu.SEMAPHORE)


def _chip_exchange_start(to_send, by_chip, name):
    def body(send_ref, buf_ref, send_sems, recv_sems, send_thru, buf_thru, token):
        x, y, c = _mesh_pos()
        me = 2 * x + y
        for k, (px, py) in enumerate(_other_chips(x, y)):
            pltpu.make_async_remote_copy(
                src_ref=send_ref.at[2 * px + py], dst_ref=buf_ref.at[me], send_sem=send_sems.at[k],
                recv_sem=recv_sems.at[k], device_id=(px, py, c), device_id_type=MESH).start()
        token[...] = jnp.zeros_like(token)

    return pl.pallas_call(
        body, name=name,
        out_shape=(pltpu.SemaphoreType.DMA((3,)), pltpu.SemaphoreType.DMA((3,)),
                   pltpu.HBM(to_send.shape, to_send.dtype), pltpu.HBM(by_chip.shape, by_chip.dtype),
                   jax.ShapeDtypeStruct((8, LANES), F32)),
        in_specs=(HBM, HBM), out_specs=(SEM, SEM, HBM, HBM, pl.BlockSpec(memory_space=pltpu.VMEM)),
        input_output_aliases={0: 2, 1: 3},
        compiler_params=pltpu.CompilerParams(has_side_effects=pltpu.SideEffectType.DATAFLOW_SIDE_EFFECTING),
    )(pltpu.with_memory_space_constraint(to_send, pltpu.HBM), pltpu.with_memory_space_constraint(by_chip, pltpu.HBM))


def _chip_exchange_wait(send_sems, recv_sems, send_thru, buf_thru, after, name):
    n_after = len(after)

    def body(send_ref, buf_ref, send_sems, recv_sems, *rest):
        x, y, c = _mesh_pos()
        me = 2 * x + y
        for k, (px, py) in enumerate(_other_chips(x, y)):
            cp = pltpu.make_async_remote_copy(
                src_ref=send_ref.at[me], dst_ref=buf_ref.at[2 * px + py], send_sem=send_sems.at[k],
                recv_sem=recv_sems.at[k], device_id=(px, py, c), device_id_type=MESH)
            cp.wait_send()
            cp.wait_recv()

    return pl.pallas_call(
        body, name=name,
        out_shape=(pltpu.HBM(send_thru.shape, send_thru.dtype), pltpu.HBM(buf_thru.shape, buf_thru.dtype)),
        in_specs=(HBM, HBM, SEM, SEM) + (ANY,) * n_after, out_specs=(HBM, HBM), input_output_aliases={0: 0, 1: 1},
        compiler_params=pltpu.CompilerParams(has_side_effects=pltpu.SideEffectType.DATAFLOW_SIDE_EFFECTING),
    )(send_thru, buf_thru, send_sems, recv_sems, *after)[1]


def _pair_gather_rider(bufs):
    n = len(bufs)

    def copies(r_out, sems):
        x, y, c = _mesh_pos()
        out = []
        for a in range(n):
            send = functools.partial(
                    pltpu.make_async_remote_copy,
                src_ref=r_out[a].at[c], dst_ref=r_out[a].at[c], send_sem=sems[0].at[a],
                recv_sem=sems[1].at[a], device_id=(x, y, 1 - c), device_id_type=MESH)
            recv = functools.partial(
                    pltpu.make_async_remote_copy,
                src_ref=r_out[a].at[1 - c], dst_ref=r_out[a].at[1 - c], send_sem=sems[0].at[a],
                recv_sem=sems[1].at[a], device_id=(x, y, 1 - c), device_id_type=MESH)
            out.append((send, recv))
        return out

    def start(r_in, r_out, sems):
        for send, _ in copies(r_out, sems):
            send().start()

    def wait(r_in, r_out, sems):
        cps = copies(r_out, sems)
        for _, recv in cps:
            recv().wait_recv()
        for send, _ in cps:
            send().wait_send()

    return _Rider(bufs, [jax.ShapeDtypeStruct(b.shape, b.dtype) for b in bufs], {i: i for i in range(n)},
                  [pltpu.SemaphoreType.DMA((n,)), pltpu.SemaphoreType.DMA((n,))], start, wait)


def _add_own_half(g, recv, pos_arr, name):
    nb, rh, cols = g.shape[0], g.shape[-2], g.shape[-1]

    def body(pos_ref, g_ref, r_ref, send_ref, own_ref):
        s = (g_ref[...] + r_ref[...]).astype(BF16)
        send_ref[...] = s

        @pl.when(pl.program_id(0) == pos_ref[1])
        def _():
            own_ref[...] = s

    blk = pl.BlockSpec((None, rh, cols), lambda j, pos_ref: (j, 0, 0))
    g_spec = blk if g.ndim == 3 else pl.BlockSpec((None, None, rh, cols),
                                                   lambda j, pos_ref: (j, pos_ref[0], 0, 0))
    shape = jax.ShapeDtypeStruct((nb, rh, cols), BF16)
    return pl.pallas_call(
        body, name=name,
        grid_spec=pltpu.PrefetchScalarGridSpec(
            num_scalar_prefetch=1, grid=(nb,), in_specs=[g_spec, blk],
            out_specs=[blk, pl.BlockSpec((None, rh, cols), lambda j, pos_ref: (pos_ref[1], 0, 0))]),
        out_shape=[shape, shape], compiler_params=_params(),
    )(pos_arr, g, recv)


def _sum_chips(gath, pos_arr, name):
    nb, rh, cols = gath.shape

    def body(pos_ref, a_ref, b_ref, c_ref, d_ref, o_ref):
        del pos_ref
        o_ref[...] = ((a_ref[...].astype(F32) + b_ref[...].astype(F32)) + c_ref[...].astype(F32)) \
            + d_ref[...].astype(F32)

    tr = rh // 2 if (rh // 2) % 16 == 0 else rh
    specs = [pl.BlockSpec((None, tr, cols), functools.partial(lambda i, pos_ref, j: (j, i, 0), j=j))
             for j in range(nb)]
    return pl.pallas_call(
        body, name=name,
        grid_spec=pltpu.PrefetchScalarGridSpec(
            num_scalar_prefetch=1, grid=(rh // tr,), in_specs=specs,
            out_specs=pl.BlockSpec((None, tr, cols), lambda i, pos_ref: (pos_ref[0], i, 0))),
        out_shape=jax.ShapeDtypeStruct((2, rh, cols), F32), compiler_params=_params(),
    )(pos_arr, gath, gath, gath, gath)


def _small_allreduce(v, name, rider=None):
    n = v.shape[0]
    n_dev = 8

    def body(v_ref, o_ref, buf, send_sems, recv_sems):
        x, y, c = _mesh_pos()
        me = 4 * x + 2 * y + c
        buf[me] = v_ref[...]
        sends = []
        peers = []
        for r in range(1, n_dev):
            px = 1 - x if r & 4 else x
            py = 1 - y if r & 2 else y
            pc = 1 - c if r & 1 else c
            peers.append((px, py, pc))
            cp = pltpu.make_async_remote_copy(
                src_ref=v_ref, dst_ref=buf.at[me], send_sem=send_sems.at[r - 1],
                recv_sem=recv_sems.at[r - 1], device_id=(px, py, pc), device_id_type=MESH)
            cp.start()
            sends.append(cp)
        for r, (px, py, pc) in enumerate(peers):
            pltpu.make_async_remote_copy(
                src_ref=v_ref, dst_ref=buf.at[4 * px + 2 * py + pc], send_sem=send_sems.at[r],
                recv_sem=recv_sems.at[r], device_id=(px, py, pc), device_id_type=MESH).wait_recv()
        for cp in sends:
            cp.wait_send()
        acc = buf[0]
        for i in range(1, n_dev):
            acc = acc + buf[i]
        o_ref[...] = acc

    whole = pl.BlockSpec(v.shape, lambda i: (0, 0))
    return _pallas(
        body, name=name, grid=(1,), in_specs=[whole], out_specs=whole,
        out_shape=jax.ShapeDtypeStruct(v.shape, v.dtype), operands=[v],
        scratch_shapes=[pltpu.VMEM((n_dev, n, LANES), F32), pltpu.SemaphoreType.DMA((n_dev - 1,)),
                        pltpu.SemaphoreType.DMA((n_dev - 1,))],
        rider=rider)


def _adamw_math(w, g, m, v):
    m = ADAM_B1 * m + (1.0 - ADAM_B1) * g
    v = ADAM_B2 * v + (1.0 - ADAM_B2) * (g * g)
    m_hat = m / (1.0 - ADAM_B1 ** ADAM_STEP)
    v_hat = v / (1.0 - ADAM_B2 ** ADAM_STEP)
    delta = -ADAM_LR * (m_hat / (jnp.sqrt(v_hat) + ADAM_EPS) + ADAM_WD * w)
    return delta, m, v


def _adamw(w, g, m, v, name, after=()):
    r, c = w.shape
    tr = 128 if r % 128 == 0 else 64
    assert r % tr == 0

    def body(w_ref, g_ref, m_ref, v_ref, go_ref, d_ref, mo_ref, vo_ref):
        gv = g_ref[...]
        d, mn, vn = _adamw_math(w_ref[...], gv, m_ref[...], v_ref[...])
        go_ref[...] = gv
        d_ref[...] = d
        mo_ref[...] = mn
        vo_ref[...] = vn

    blk = pl.BlockSpec((tr, c), lambda i: (i, 0))
    return _pallas(body, name=name, grid=(r // tr,), in_specs=[blk] * 4, out_specs=[blk] * 4,
                   out_shape=[jax.ShapeDtypeStruct((r, c), F32)] * 4, operands=[w, g, m, v], after=after)


def _adamw_small(ws, gs, ms, vs, name):
    n = len(ws)

    def body(*refs):
        w_r, g_r, m_r, v_r = refs[:n], refs[n:2 * n], refs[2 * n:3 * n], refs[3 * n:4 * n]
        d_o, m_o, v_o = refs[4 * n:5 * n], refs[5 * n:6 * n], refs[6 * n:7 * n]
        for i in range(n):
            d, mn, vn = _adamw_math(w_r[i][...], g_r[i][...], m_r[i][...], v_r[i][...])
            d_o[i][...] = d
            m_o[i][...] = mn
            v_o[i][...] = vn

    specs = [pl.BlockSpec(w.shape, lambda i: (0, 0)) for w in ws]
    shapes = [jax.ShapeDtypeStruct(w.shape, F32) for w in ws]
    outs = pl.pallas_call(
        body, name=name, grid=(1,), in_specs=specs * 4, out_specs=specs * 3, out_shape=shapes * 3,
        compiler_params=_params(),
    )(*ws, *gs, *ms, *vs)
    return outs[:n], outs[n:2 * n], outs[2 * n:]


BIG = ("w_in", "w_o_attn", "w_pw_conv", "w_out", "w_ffn_in", "w_ffn_out")
ROW_SHARDED = ("w_out", "w_ffn_out")
SMALL = ("norm1_w", "b_gate", "q_norm_w", "k_norm_w", "conv_w", "conv_b", "conv_ln_w", "conv_ln_b", "norm2_w")
ORDER = ("norm1_w", "w_in", "b_gate", "q_norm_w", "k_norm_w", "w_o_attn", "conv_w", "conv_b", "conv_ln_w",
         "conv_ln_b", "w_pw_conv", "w_out", "norm2_w", "w_ffn_in", "w_ffn_out")
PACK_TILE = 8 * LANES


def _pack_small(parts):
    rows = []
    for p in parts:
        flat = p.reshape(-1)
        pad = (-flat.shape[0]) % PACK_TILE
        rows.append(jnp.pad(flat, (0, pad)).reshape(-1, LANES))
    return jnp.concatenate(rows, axis=0)


def _unpack_small(packed, shapes):
    out, row = [], 0
    for shp in shapes:
        size = int(np.prod(shp))
        nrow = -(-size // PACK_TILE) * (PACK_TILE // LANES)
        out.append(packed[row:row + nrow].reshape(-1)[:size].reshape(shp))
        row += nrow
    return out


def kernel(x, positions, norm1_w, w_in, b_gate, q_norm_w, k_norm_w, w_o_attn, conv_w, conv_b, conv_ln_w, conv_ln_b, w_pw_conv, w_out, norm2_w, w_ffn_in, w_ffn_out, loss_target, m_norm1_w, m_w_in, m_b_gate, m_q_norm_w, m_k_norm_w, m_w_o_attn, m_conv_w, m_conv_b, m_conv_ln_w, m_conv_ln_b, m_w_pw_conv, m_w_out, m_norm2_w, m_w_ffn_in, m_w_ffn_out, v_norm1_w, v_w_in, v_b_gate, v_q_norm_w, v_k_norm_w, v_w_o_attn, v_conv_w, v_conv_b, v_conv_ln_w, v_conv_ln_b, v_w_pw_conv, v_w_out, v_norm2_w, v_w_ffn_in, v_w_ffn_out):
    w = dict(norm1_w=norm1_w, w_in=w_in, b_gate=b_gate, q_norm_w=q_norm_w, k_norm_w=k_norm_w, w_o_attn=w_o_attn,
             conv_w=conv_w, conv_b=conv_b, conv_ln_w=conv_ln_w, conv_ln_b=conv_ln_b, w_pw_conv=w_pw_conv,
             w_out=w_out, norm2_w=norm2_w, w_ffn_in=w_ffn_in, w_ffn_out=w_ffn_out)
    m = dict(norm1_w=m_norm1_w, w_in=m_w_in, b_gate=m_b_gate, q_norm_w=m_q_norm_w, k_norm_w=m_k_norm_w,
             w_o_attn=m_w_o_attn, conv_w=m_conv_w, conv_b=m_conv_b, conv_ln_w=m_conv_ln_w,
             conv_ln_b=m_conv_ln_b, w_pw_conv=m_w_pw_conv, w_out=m_w_out, norm2_w=m_norm2_w,
             w_ffn_in=m_w_ffn_in, w_ffn_out=m_w_ffn_out)
    v = dict(norm1_w=v_norm1_w, w_in=v_w_in, b_gate=v_b_gate, q_norm_w=v_q_norm_w, k_norm_w=v_k_norm_w,
             w_o_attn=v_w_o_attn, conv_w=v_conv_w, conv_b=v_conv_b, conv_ln_w=v_conv_ln_w,
             conv_ln_b=v_conv_ln_b, w_pw_conv=v_w_pw_conv, w_out=v_w_out, norm2_w=v_norm2_w,
             w_ffn_in=v_w_ffn_in, w_ffn_out=v_w_ffn_out)
    cx, cy, cc = _mesh_pos()
    chip = 2 * cx + cy

    chip_arr = chip.reshape(1).astype(jnp.int32)
    pos_arr = jnp.stack([cc, chip]).astype(jnp.int32)
    bufs = {}
    for n in BIG:
        buf = _cast_into_slot(w[n][0], chip_arr, BF16, f"cast_{n}")
        bufs[n] = buf.reshape(N_CHIPS, 2, buf.shape[1] // 2, buf.shape[2])
    small_bufs = [_cast_into_slot(w[n][0], chip_arr, F32, f"slot_{n}") for n in ("conv_w", "b_gate")]
    w_in_buf, conv_w_buf, b_gate_buf = _allgather_inplace([bufs["w_in"]], small_bufs, "allgather_w_in")
    wts = dict(w_in=w_in_buf.reshape(N_CHIPS, -1, w_in_buf.shape[3]),
               conv_w=conv_w_buf.transpose(1, 0, 2).reshape(CONV_WIDTH, -1),
               b_gate=b_gate_buf.transpose(1, 0, 2).reshape(2, 1, -1),
               norm1_w=norm1_w, q_norm_w=q_norm_w, k_norm_w=k_norm_w, conv_b=conv_b, conv_ln_w=conv_ln_w,
               conv_ln_b=conv_ln_b, norm2_w=norm2_w)

    loss, grad_x, g, reduced, w_in_in_flight = _forward_backward(
        x[0], positions.reshape(-1, 1), loss_target[0], wts, [bufs[n] for n in LATE_GATHER], pos_arr)
    grads = {n: b.reshape(-1, b.shape[2]) for n, b in reduced.items()}

    *w_in_in_flight, started = w_in_in_flight
    delta, new_m, new_v = {}, {}, {}
    for n in EARLY_REDUCE:
        grads[n], delta[n], new_m[n], new_v[n] = _adamw(w[n][0], grads[n], m[n][0], v[n][0], f"adamw_{n}",
                                                        after=[started])
    small_parts = [loss] + [g[n] for n in SMALL]
    small_shapes = [p.shape for p in small_parts]
    summed = _small_allreduce(_pack_small(small_parts), "small_allreduce")
    reduced = _unpack_small(summed, small_shapes)
    loss_total = reduced[0].reshape(())
    for n, r in zip(SMALL, reduced[1:]):
        grads[n] = r
    ch_shard = conv_w.shape[2]
    grads["conv_w"] = lax.dynamic_slice_in_dim(grads["conv_w"], chip * ch_shard, ch_shard, axis=1)
    d_shard = b_gate.shape[2]
    grads["b_gate"] = lax.dynamic_slice_in_dim(grads["b_gate"], chip * d_shard, d_shard, axis=1)

    by_chip_w_in = _chip_exchange_wait(*w_in_in_flight, after=[delta[n] for n in EARLY_REDUCE] + [summed],
                                       name="grads_w_in_exchange_wait")
    half_w_in = _sum_chips(by_chip_w_in, pos_arr, "grads_chip_sum_w_in")
    (shard_w_in,) = _comm_call(_pair_gather_rider([half_w_in]), "grads_pair_gather_w_in")
    grads["w_in"], delta["w_in"], new_m["w_in"], new_v["w_in"] = _adamw(
        w["w_in"][0], shard_w_in.reshape(-1, shard_w_in.shape[2]), m["w_in"][0], v["w_in"][0], "adamw_w_in")
    flat2 = lambda a: a.reshape(-1, a.shape[-1])
    d_s, m_s, v_s = _adamw_small([flat2(w[n]) for n in SMALL], [flat2(grads[n]) for n in SMALL],
                                 [flat2(m[n]) for n in SMALL], [flat2(v[n]) for n in SMALL], "adamw_small")
    for i, n in enumerate(SMALL):
        delta[n], new_m[n], new_v[n] = d_s[i], m_s[i], v_s[i]

    shaped = lambda d, n: d[n].reshape(w[n].shape)
    return (loss_total, grad_x[None], *[shaped(grads, n) for n in ORDER], *[shaped(delta, n) for n in ORDER],
            *[shaped(new_m, n) for n in ORDER], *[shaped(new_v, n) for n in ORDER])
```

```python
import functools

import numpy as np
import jax
import jax.numpy as jnp
from jax import lax
from jax.experimental import pallas as pl
from jax.experimental.pallas import tpu as pltpu

F32 = jnp.float32
BF16 = jnp.bfloat16
MESH = pl.DeviceIdType.MESH
ANY = pl.BlockSpec(memory_space=pl.ANY)

HEAD_DIM = 64
N_SLOT_HEADS = 8
DILATIONS = (1, 4, 16)
HALF_SPAN = 64
ROPE_THETA = 500000.0
ROT_DIM = 16
CONV_WIDTH = 31
EPS = 1e-6
NEG_INF = -1e30
ADAM_LR, ADAM_B1, ADAM_B2, ADAM_EPS, ADAM_WD, ADAM_STEP = 0.001, 0.9, 0.999, 1e-08, 0.01, 10

LANES = 128
QBLK = 128
KWIN = QBLK + 2 * HALF_SPAN
VMEM_LIMIT = 48 * 1024 * 1024
N_CHIPS = 4


def _params(**kw):
    return pltpu.CompilerParams(vmem_limit_bytes=VMEM_LIMIT, **kw)


class _Rider:
    def __init__(self, operands, out_shapes, aliases, scratch, start, wait):
        self.operands, self.out_shapes, self.aliases = list(operands), list(out_shapes), dict(aliases)
        self.scratch, self.start, self.wait = list(scratch), start, wait


def _pallas(body, *, name, grid, in_specs, out_specs, out_shape, operands, scratch_shapes=(), aliases=None,
            rider=None, after=()):
    single = not isinstance(out_specs, (list, tuple))
    out_specs_l = [out_specs] if single else list(out_specs)
    out_shape_l = [out_shape] if single else list(out_shape)
    aliases = dict(aliases or {})
    if rider is None:
        n_main = len(in_specs)

        def ordered(*refs):
            body(*refs[:n_main], *refs[n_main + len(after):])

        res = pl.pallas_call(
            ordered if after else body, name=name, grid=grid, in_specs=list(in_specs) + [ANY] * len(after),
            out_specs=out_specs_l, out_shape=out_shape_l, scratch_shapes=list(scratch_shapes),
            input_output_aliases=aliases, compiler_params=_params(),
        )(*operands, *after)
        return res[0] if single else res
    assert not after
    n_in, n_rin = len(in_specs), len(rider.operands)
    n_out, n_rout = len(out_specs_l), len(rider.out_shapes)
    n_sc = len(scratch_shapes)

    def wrapped(*refs):
        main_in, r_in = refs[:n_in], refs[n_in:n_in + n_rin]
        o0 = n_in + n_rin
        main_out, r_out = refs[o0:o0 + n_out], refs[o0 + n_out:o0 + n_out + n_rout]
        s0 = o0 + n_out + n_rout
        main_sc, r_sc = refs[s0:s0 + n_sc], refs[s0 + n_sc:]
        ids = [pl.program_id(d) for d in range(len(grid))]
        first = functools.reduce(jnp.logical_and, [i == 0 for i in ids])
        last = functools.reduce(jnp.logical_and, [i == n - 1 for i, n in zip(ids, grid)])

        @pl.when(first)
        def _():
            rider.start(r_in, r_out, r_sc)

        body(*main_in, *main_out, *main_sc)

        @pl.when(last)
        def _():
            rider.wait(r_in, r_out, r_sc)

    for src, dst in rider.aliases.items():
        aliases[n_in + src] = n_out + dst
    res = pl.pallas_call(
        wrapped, name=name, grid=grid, in_specs=list(in_specs) + [ANY] * n_rin,
        out_specs=out_specs_l + [ANY] * n_rout, out_shape=out_shape_l + rider.out_shapes,
        scratch_shapes=list(scratch_shapes) + rider.scratch, input_output_aliases=aliases,
        compiler_params=_params(),
    )(*operands, *rider.operands)
    main = res[:n_out]
    return (main[0] if single else main), res[n_out:]


def _matmul(a, b, *, mode, tm, tn, tk, out_dtype, name, b_blocked=False,
            out_blocked=None, rider=None, after=()):
    a_shape = a.shape
    if mode == "nn":
        m_dim, k_dim = a_shape
        n_dim = b.shape[0] * b.shape[2] if b_blocked else b.shape[1]
        rows, cols, red = m_dim, n_dim, k_dim
    elif mode == "nt":
        m_dim, n_dim = a_shape
        k_dim = b.shape[1] if b_blocked else b.shape[0]
        rows, cols, red = m_dim, k_dim, n_dim
    else:
        m_dim, k_dim = a_shape
        n_dim = b.shape[1]
        rows, cols, red = k_dim, n_dim, m_dim
    assert rows % tm == 0 and cols % tn == 0 and red % tk == 0, (name, rows, cols, red)
    ni, nj, nk = rows // tm, cols // tn, red // tk

    if mode == "nn":
        a_spec = pl.BlockSpec((tm, tk), lambda i, j, k: (i, k))
        if b_blocked:
            per = b.shape[2] // tn
            b_spec = pl.BlockSpec((None, tk, tn), lambda i, j, k: (j // per, k, j % per))
        else:
            b_spec = pl.BlockSpec((tk, tn), lambda i, j, k: (k, j))
        dims = (((1,), (0,)), ((), ()))
    elif mode == "nt":
        a_spec = pl.BlockSpec((tm, tk), lambda i, j, k: (i, k))
        if b_blocked:
            per = b.shape[2] // tk
            b_spec = pl.BlockSpec((None, tn, tk), lambda i, j, k: (k // per, j, k % per))
        else:
            b_spec = pl.BlockSpec((tn, tk), lambda i, j, k: (j, k))
        dims = (((1,), (1,)), ((), ()))
    else:
        a_spec = pl.BlockSpec((tk, tm), lambda i, j, k: (k, i))
        b_spec = pl.BlockSpec((tk, tn), lambda i, j, k: (k, j))
        dims = (((0,), (0,)), ((), ()))

    if out_blocked:
        per_o = (cols // out_blocked) // tn
        out_spec = pl.BlockSpec((None, tm, tn), lambda i, j, k: (j // per_o, i, j % per_o))
        out_shape = jax.ShapeDtypeStruct((out_blocked, rows, cols // out_blocked), out_dtype)
    else:
        out_spec = pl.BlockSpec((tm, tn), lambda i, j, k: (i, j))
        out_shape = jax.ShapeDtypeStruct((rows, cols), out_dtype)

    def body(a_ref, b_ref, o_ref, *acc):
        prod = lax.dot_general(a_ref[...], b_ref[...], dims, preferred_element_type=F32)
        if nk == 1:
            o_ref[...] = prod.astype(out_dtype)
        else:
            acc_ref, = acc
            k = pl.program_id(2)

            @pl.when(k == 0)
            def _():
                acc_ref[...] = prod

            @pl.when(k > 0)
            def _():
                acc_ref[...] += prod

            @pl.when(k == nk - 1)
            def _():
                o_ref[...] = acc_ref[...].astype(out_dtype)

    scratch = [pltpu.VMEM((tm, tn), F32)] if nk > 1 else []
    return _pallas(body, name=name, grid=(ni, nj, nk), in_specs=[a_spec, b_spec], out_specs=out_spec,
                   out_shape=out_shape, operands=[a, b], scratch_shapes=scratch, rider=rider, after=after)


def _rmsnorm_fwd(x, w, name):
    s, d = x.shape
    tm = 256

    def body(x_ref, w_ref, o_ref):
        xv = x_ref[...]
        rstd = lax.rsqrt(jnp.mean(xv * xv, axis=-1, keepdims=True) + EPS)
        o_ref[...] = (xv * rstd * w_ref[...]).astype(BF16)

    return pl.pallas_call(
        body, name=name, grid=(s // tm,),
        in_specs=[pl.BlockSpec((tm, d), lambda i: (i, 0)), pl.BlockSpec((1, d), lambda i: (0, 0))],
        out_specs=pl.BlockSpec((tm, d), lambda i: (i, 0)),
        out_shape=jax.ShapeDtypeStruct((s, d), BF16), compiler_params=_params(),
    )(x, w)


def _rmsnorm_bwd(dh, x, w, dres, name, rider=None):
    s, d = x.shape
    tm = 256

    def body(dh_ref, x_ref, w_ref, dres_ref, dx_ref, dxb_ref, dw_ref):
        xv = x_ref[...]
        rstd = lax.rsqrt(jnp.mean(xv * xv, axis=-1, keepdims=True) + EPS)
        xhat = xv * rstd
        dhv = dh_ref[...]
        g = dhv * w_ref[...]
        dx = rstd * (g - xhat * jnp.mean(g * xhat, axis=-1, keepdims=True)) + dres_ref[...]
        dx_ref[...] = dx
        dxb_ref[...] = dx.astype(BF16)
        part = jnp.sum(dhv * xhat, axis=0, keepdims=True)

        @pl.when(pl.program_id(0) == 0)
        def _():
            dw_ref[...] = part

        @pl.when(pl.program_id(0) > 0)
        def _():
            dw_ref[...] += part

    row = pl.BlockSpec((tm, d), lambda i: (i, 0))
    vec = pl.BlockSpec((1, d), lambda i: (0, 0))
    return _pallas(
        body, name=name, grid=(s // tm,), in_specs=[row, row, vec, row], out_specs=[row, row, vec],
        out_shape=[jax.ShapeDtypeStruct((s, d), F32), jax.ShapeDtypeStruct((s, d), BF16),
                   jax.ShapeDtypeStruct((1, d), F32)],
        operands=[dh, x, w, dres], rider=rider)


def _rope_consts():
    lane = np.arange(LANES)
    in_head = lane % HEAD_DIM
    inv_freq = ROPE_THETA ** (-jnp.arange(0, ROT_DIM, 2, dtype=F32) / ROT_DIM)
    invf = jnp.where(jnp.asarray(in_head < ROT_DIM), jnp.tile(inv_freq, LANES // (ROT_DIM // 2)), 0.0)
    m_a = np.where(in_head < ROT_DIM // 2, -1.0, 0.0).astype(np.float32)
    m_b = np.where((in_head >= ROT_DIM // 2) & (in_head < ROT_DIM), 1.0, 0.0).astype(np.float32)
    block_diag = (lane[:, None] // HEAD_DIM == lane[None, :] // HEAD_DIM).astype(np.float32)
    return (invf.reshape(1, LANES).astype(F32), jnp.asarray(m_a).reshape(1, LANES),
            jnp.asarray(m_b).reshape(1, LANES), jnp.asarray(block_diag, dtype=BF16))


def _head_sums(v, bd):
    hi = v.astype(BF16)
    lo = (v - hi.astype(F32)).astype(BF16)
    return jnp.dot(hi, bd, preferred_element_type=F32) + jnp.dot(lo, bd, preferred_element_type=F32)


def _qk_fwd(proj, pos_col, qw2, kw2, consts, name, rider=None):
    s = proj.shape[0]
    width = 3 * N_SLOT_HEADS * HEAD_DIM
    tm = 128
    invf, m_a, m_b, bd = consts
    scale = HEAD_DIM ** -0.5

    def body(q_ref, k_ref, pos_ref, qw_ref, kw_ref, invf_ref, ma_ref, mb_ref, bd_ref, qo_ref, ko_ref):
        ang = pos_ref[...].astype(F32) * invf_ref[...]
        cos = jnp.cos(ang)
        sin = jnp.sin(ang)
        s_a = sin * ma_ref[...]
        s_b = sin * mb_ref[...]
        bdv = bd_ref[...]
        for src, w_ref, dst, sc in ((q_ref, qw_ref, qo_ref, scale), (k_ref, kw_ref, ko_ref, 1.0)):
            for cb in range(width // LANES):
                cols = slice(cb * LANES, (cb + 1) * LANES)
                t = src[:, cols]
                rstd = lax.rsqrt(_head_sums(t * t, bdv) * (1.0 / HEAD_DIM) + EPS)
                y = t * rstd * w_ref[...]
                r = y * cos + pltpu.roll(y, LANES - 8, axis=1) * s_a + pltpu.roll(y, 8, axis=1) * s_b
                dst[:, cols] = r * sc if sc != 1.0 else r

    vec = pl.BlockSpec((1, LANES), lambda i: (0, 0))
    return _pallas(
        body, name=name, grid=(s // tm,),
        in_specs=[pl.BlockSpec((tm, width), lambda i: (i, 0)), pl.BlockSpec((tm, width), lambda i: (i, 1)),
                  pl.BlockSpec((tm, 1), lambda i: (i, 0)), vec, vec, vec, vec, vec,
                  pl.BlockSpec((LANES, LANES), lambda i: (0, 0))],
        out_specs=[pl.BlockSpec((tm, width), lambda i: (i, 0))] * 2,
        out_shape=[jax.ShapeDtypeStruct((s, width), F32)] * 2,
        operands=[proj, proj, pos_col, qw2, kw2, invf, m_a, m_b, bd], rider=rider)


def _qk_bwd(dqn, dkn, dv, da, db, dgl, proj, pos_col, qw2, kw2, consts, name, rider=None):
    s = proj.shape[0]
    width = 3 * N_SLOT_HEADS * HEAD_DIM
    ch = da.shape[1]
    gate_w = dgl.shape[1]
    out_w = 3 * width + 2 * ch + gate_w
    assert out_w == proj.shape[1]
    tm = 128
    invf, m_a, m_b, bd = consts
    scale = HEAD_DIM ** -0.5

    def body(dq_ref, dk_ref, dv_ref, da_ref, db_ref, dgl_ref, q_ref, k_ref, pos_ref, qw_ref, kw_ref,
             invf_ref, ma_ref, mb_ref, bd_ref, out_ref, dqw_ref, dkw_ref):
        ang = pos_ref[...].astype(F32) * invf_ref[...]
        cos = jnp.cos(ang)
        sin = jnp.sin(ang)
        s_a = sin * ma_ref[...]
        s_b = sin * mb_ref[...]
        bdv = bd_ref[...]
        first = pl.program_id(0) == 0
        for src, dsrc, w_ref, col0, dw_ref, sc in ((q_ref, dq_ref, qw_ref, 0, dqw_ref, scale),
                                                   (k_ref, dk_ref, kw_ref, width, dkw_ref, 1.0)):
            dw_acc = jnp.zeros((1, LANES), F32)
            for cb in range(width // LANES):
                cols = slice(cb * LANES, (cb + 1) * LANES)
                t = src[:, cols]
                dr = dsrc[:, cols]
                if sc != 1.0:
                    dr = dr * sc
                dy = dr * cos + pltpu.roll(dr * s_a, 8, axis=1) + pltpu.roll(dr * s_b, LANES - 8, axis=1)
                rstd = lax.rsqrt(_head_sums(t * t, bdv) * (1.0 / HEAD_DIM) + EPS)
                xhat = t * rstd
                g = dy * w_ref[...]
                dt = rstd * (g - xhat * (_head_sums(g * xhat, bdv) * (1.0 / HEAD_DIM)))
                out_ref[:, col0 + cb * LANES: col0 + (cb + 1) * LANES] = dt.astype(BF16)
                dw_acc = dw_acc + jnp.sum(dy * xhat, axis=0, keepdims=True)
            dw_acc = dw_acc + pltpu.roll(dw_acc, HEAD_DIM, axis=1)

            @pl.when(first)
            def _(dw_ref=dw_ref, dw_acc=dw_acc):
                dw_ref[...] = dw_acc

            @pl.when(jnp.logical_not(first))
            def _(dw_ref=dw_ref, dw_acc=dw_acc):
                dw_ref[...] += dw_acc
        out_ref[:, 2 * width: 3 * width] = dv_ref[...].astype(BF16)
        out_ref[:, 3 * width: 3 * width + ch] = da_ref[...]
        out_ref[:, 3 * width + ch: 3 * width + 2 * ch] = db_ref[...]
        out_ref[:, 3 * width + 2 * ch: out_w] = dgl_ref[...]

    vec = pl.BlockSpec((1, LANES), lambda i: (0, 0))
    blk = lambda c: pl.BlockSpec((tm, width), lambda i: (i, c))
    cblk = pl.BlockSpec((tm, ch), lambda i: (i, 0))
    return _pallas(
        body, name=name, grid=(s // tm,),
        in_specs=[blk(0), blk(0), blk(0), cblk, cblk, pl.BlockSpec((tm, gate_w), lambda i: (i, 0)),
                  blk(0), blk(1), pl.BlockSpec((tm, 1), lambda i: (i, 0)), vec, vec, vec, vec, vec,
                  pl.BlockSpec((LANES, LANES), lambda i: (0, 0))],
        out_specs=[pl.BlockSpec((tm, out_w), lambda i: (i, 0)), vec, vec],
        out_shape=[jax.ShapeDtypeStruct((s, out_w), BF16)] + [jax.ShapeDtypeStruct((1, LANES), F32)] * 2,
        operands=[dqn, dkn, dv, da, db, dgl, proj, proj, pos_col, qw2, kw2, invf, m_a, m_b, bd],
        rider=rider)


def _row_chunks(n_rows, fn, chunk=256):
    def step(i, c):
        fn(pl.ds(pl.multiple_of(i * chunk, chunk), chunk))
        return c
    lax.fori_loop(0, n_rows // chunk, step, 0)


def _to_residue_major(dst, src, s, d, dst_off=0, cast=None):
    seq = s // d
    for r in range(d):
        v = src[...] if d == 1 else src[pl.ds(r, seq, stride=d), :]
        dst[dst_off + r * seq: dst_off + (r + 1) * seq, :] = v if cast is None else v.astype(cast)


def _from_residue_major(dst, src, s, d, src_off=0):
    seq = s // d
    for r in range(d):
        v = src[src_off + r * seq: src_off + (r + 1) * seq, :]
        if d == 1:
            dst[...] = v
        else:
            dst[pl.ds(r, seq, stride=d), :] = v


def _band_bias():
    qi = lax.broadcasted_iota(jnp.int32, (QBLK, KWIN), 0)
    kj = lax.broadcasted_iota(jnp.int32, (QBLK, KWIN), 1)
    return jnp.where(jnp.abs(kj - HALF_SPAN - qi) <= HALF_SPAN, 0.0, NEG_INF).astype(F32)


def _range_bias(base, seq):
    kj = lax.broadcasted_iota(jnp.int32, (1, KWIN), 1)
    lo = (base & -seq) - base + HALF_SPAN
    return jnp.where((kj >= lo) & (kj < lo + seq), 0.0, NEG_INF).astype(F32)


def _block_base(b):
    return b * QBLK if isinstance(b, int) else pl.multiple_of(b * QBLK, QBLK)


def _attn_fwd(qn, kn, proj, name, rider=None):
    s = qn.shape[0]
    n_pairs = N_SLOT_HEADS * HEAD_DIM // LANES
    v_col0 = 2 * qn.shape[1] // LANES
    nt_dims = (((1,), (1,)), ((), ()))

    def body(q_ref, k_ref, v_ref, attn_ref, lse_ref, attn_b_ref, q_rm, k_rm, v_rm, acc_rm, m_rm, l_rm,
             acc_p, m_p, l_p, m_run, l_run, acc_run, band, s_buf, m_buf):
        g = pl.program_id(1)
        zpad = jnp.zeros((HALF_SPAN, LANES), BF16)
        k_rm[0:HALF_SPAN, :] = zpad
        k_rm[s + HALF_SPAN: s + 2 * HALF_SPAN, :] = zpad
        v_rm[0:HALF_SPAN, 0:LANES] = zpad
        v_rm[s + HALF_SPAN: s + 2 * HALF_SPAN, 0:LANES] = zpad

        def ones_rows(rows):
            v_rm[pl.ds(rows.start, rows.size), LANES:2 * LANES] = jnp.ones((rows.size, LANES), BF16)

        _row_chunks(s + 2 * HALF_SPAN, ones_rows, chunk=2 * HALF_SPAN)
        band[...] = _band_bias()
        lane = lax.broadcasted_iota(jnp.int32, (QBLK, LANES), 1)
        low = lane < HEAD_DIM
        n_blk = s // QBLK

        for gi, d in enumerate(DILATIONS):
            @pl.when(g == gi)
            def _(gi=gi, d=d):
                seq = s // d
                _to_residue_major(q_rm, q_ref, s, d, cast=BF16)
                _to_residue_major(k_rm, k_ref, s, d, dst_off=HALF_SPAN, cast=BF16)
                _to_residue_major(v_rm.at[:, 0:LANES], v_ref, s, d, dst_off=HALF_SPAN, cast=BF16)

                def scores(b, slot):
                    base = _block_base(b)
                    q = q_rm[pl.ds(base, QBLK), :]
                    zero = jnp.zeros_like(q)
                    q2 = jnp.concatenate([jnp.where(low, q, zero), jnp.where(low, zero, q)], axis=0)
                    sc = lax.dot_general(q2, k_rm[pl.ds(base, KWIN), :], nt_dims, preferred_element_type=F32)
                    bias = band[...] + _range_bias(base, seq)
                    for hh in range(2):
                        rows = slice(hh * QBLK, (hh + 1) * QBLK)
                        sh = sc[rows, :] + bias
                        s_buf[slot, rows, :] = sh
                        m_buf[slot, rows, :] = jnp.broadcast_to(jnp.max(sh, axis=-1, keepdims=True), (QBLK, LANES))

                def outputs(b, slot):
                    base = _block_base(b)
                    sv = s_buf[slot]
                    mb = m_buf[slot]
                    p = jnp.exp(jnp.concatenate([sv[:, 0:LANES] - mb, sv[:, LANES:2 * LANES] - mb], axis=1))
                    pv = jnp.dot(p.astype(BF16), v_rm[pl.ds(base, KWIN), :], preferred_element_type=F32)
                    rows = pl.ds(base, QBLK)
                    acc_rm[rows, :] = jnp.where(low, pv[0:QBLK, 0:LANES], pv[QBLK:2 * QBLK, 0:LANES])
                    l_rm[rows, :] = jnp.where(low, pv[0:QBLK, LANES:2 * LANES], pv[QBLK:2 * QBLK, LANES:2 * LANES])
                    m_rm[rows, :] = jnp.where(low, mb[0:QBLK, :], mb[QBLK:2 * QBLK, :])

                scores(0, 0)

                def pair(i, carry):
                    b = 2 * i
                    outputs(b, 0)
                    scores(b + 1, 1)
                    outputs(b + 1, 1)
                    scores(b + 2, 0)
                    return carry

                lax.fori_loop(0, n_blk // 2 - 1, pair, 0)
                outputs(n_blk - 2, 0)
                scores(n_blk - 1, 1)
                outputs(n_blk - 1, 1)
                if d == 1:
                    src = (acc_rm, m_rm, l_rm)
                else:
                    for dst_, src_ in ((acc_p, acc_rm), (m_p, m_rm), (l_p, l_rm)):
                        _from_residue_major(dst_, src_, s, d)
                    src = (acc_p, m_p, l_p)

                def combine(rows):
                    a_g, m_g, l_g = src[0][rows, :], src[1][rows, :], src[2][rows, :]
                    if gi == 0:
                        m_new, l_new, a_new = m_g, l_g, a_g
                    else:
                        m_old = m_run[rows, :]
                        m_new = jnp.maximum(m_old, m_g)
                        w_old = jnp.exp(m_old - m_new)
                        w_g = jnp.exp(m_g - m_new)
                        l_new = l_run[rows, :] * w_old + l_g * w_g
                        a_new = acc_run[rows, :] * w_old + a_g * w_g
                    if gi == len(DILATIONS) - 1:
                        out = a_new / l_new
                        attn_ref[rows, :] = out
                        attn_b_ref[rows, :] = out.astype(BF16)
                        lse_ref[rows, :] = m_new + jnp.log(l_new)
                    else:
                        m_run[rows, :] = m_new
                        l_run[rows, :] = l_new
                        acc_run[rows, :] = a_new

                _row_chunks(s, combine)

    qk_spec = pl.BlockSpec((s, LANES), lambda hp, g: (0, g * n_pairs + hp))
    v_spec = pl.BlockSpec((s, LANES), lambda hp, g: (0, v_col0 + g * n_pairs + hp))
    o_spec = pl.BlockSpec((s, LANES), lambda hp, g: (0, hp))
    f32buf = pltpu.VMEM((s, LANES), F32)
    return _pallas(
        body, name=name, grid=(n_pairs, len(DILATIONS)), in_specs=[qk_spec, qk_spec, v_spec],
        out_specs=[o_spec, o_spec, o_spec],
        out_shape=[jax.ShapeDtypeStruct((s, n_pairs * LANES), F32)] * 2
        + [jax.ShapeDtypeStruct((s, n_pairs * LANES), BF16)],
        operands=[qn, kn, proj],
        scratch_shapes=[pltpu.VMEM((s, LANES), BF16), pltpu.VMEM((s + 2 * HALF_SPAN, LANES), BF16),
                        pltpu.VMEM((s + 2 * HALF_SPAN, 2 * LANES), BF16)] + [f32buf] * 9
        + [pltpu.VMEM((QBLK, KWIN), F32), pltpu.VMEM((2, 2 * QBLK, KWIN), F32),
           pltpu.VMEM((2, 2 * QBLK, LANES), F32)],
        rider=rider)


def _attn_bwd(qn, kn, proj, dattn, attn, lse, bd, name, rider=None):
    s = qn.shape[0]
    n_pairs = N_SLOT_HEADS * HEAD_DIM // LANES
    v_col0 = 2 * qn.shape[1] // LANES
    nt_dims = (((1,), (1,)), ((), ()))
    tn_dims = (((0,), (0,)), ((), ()))
    spad = s + 2 * HALF_SPAN

    def body(q_ref, k_ref, v_ref, do_ref, o_ref, lse_ref, bd_ref, dq_ref, dk_ref, dv_ref,
             q_rm, k_rm, v_rm, do_rm, lse0_rm, lse1_rm, dd0_rm, dd1_rm, dq_rm, dk_rm, dv_rm,
             lse0_p, lse1_p, dd0_p, dd1_p, band, p_buf, ds_buf):
        g = pl.program_id(1)
        zpad = jnp.zeros((HALF_SPAN, LANES), BF16)
        for buf in (k_rm, v_rm):
            buf[0:HALF_SPAN, :] = zpad
            buf[s + HALF_SPAN: spad, :] = zpad
        zf = jnp.zeros((HALF_SPAN, LANES), F32)
        for buf in (dk_rm, dv_rm):
            buf[0:HALF_SPAN, :] = zf
            buf[s + HALF_SPAN: spad, :] = zf
        band[...] = _band_bias()

        def clear(rows):
            z = jnp.zeros((rows.size, LANES), F32)
            dk_rm[pl.ds(rows.start + HALF_SPAN, rows.size), :] = z
            dv_rm[pl.ds(rows.start + HALF_SPAN, rows.size), :] = z

        _row_chunks(s, clear)

        def prepare(rows):
            lo = lax.broadcasted_iota(jnp.int32, (rows.size, LANES), 1) < HEAD_DIM
            dsum = _head_sums(do_ref[rows, :] * o_ref[rows, :], bd_ref[...])
            dswap = pltpu.roll(dsum, HEAD_DIM, axis=1)
            dd0_p[rows, :] = jnp.where(lo, dsum, dswap)
            dd1_p[rows, :] = jnp.where(lo, dswap, dsum)
            lv = lse_ref[rows, :]
            lswap = pltpu.roll(lv, HEAD_DIM, axis=1)
            lse0_p[rows, :] = jnp.where(lo, lv, lswap)
            lse1_p[rows, :] = jnp.where(lo, lswap, lv)

        @pl.when(g == 0)
        def _():
            _row_chunks(s, prepare)
        lane = lax.broadcasted_iota(jnp.int32, (QBLK, LANES), 1)
        low = lane < HEAD_DIM
        n_blk = s // QBLK

        def stacked(ref, rows):
            val = ref[rows, :]
            zero = jnp.zeros_like(val)
            return jnp.concatenate([jnp.where(low, val, zero), jnp.where(low, zero, val)], axis=0)

        for gi, d in enumerate(DILATIONS):
            @pl.when(g == gi)
            def _(d=d):
                seq = s // d
                _to_residue_major(q_rm, q_ref, s, d, cast=BF16)
                _to_residue_major(k_rm, k_ref, s, d, dst_off=HALF_SPAN, cast=BF16)
                _to_residue_major(v_rm, v_ref, s, d, dst_off=HALF_SPAN, cast=BF16)
                _to_residue_major(do_rm, do_ref, s, d, cast=BF16)
                for dst_, src_ in ((lse0_rm, lse0_p), (lse1_rm, lse1_p), (dd0_rm, dd0_p), (dd1_rm, dd1_p)):
                    _to_residue_major(dst_, src_, s, d)

                def scores(b, slot):
                    base = _block_base(b)
                    rows = pl.ds(base, QBLK)
                    win = pl.ds(base, KWIN)
                    sc = lax.dot_general(stacked(q_rm, rows), k_rm[win, :], nt_dims, preferred_element_type=F32)
                    dp = lax.dot_general(stacked(do_rm, rows), v_rm[win, :], nt_dims, preferred_element_type=F32)
                    bias = band[...] + _range_bias(base, seq)
                    for hh, (lse_r, dd_r) in enumerate(((lse0_rm, dd0_rm), (lse1_rm, dd1_rm))):
                        r = slice(hh * QBLK, (hh + 1) * QBLK)
                        lse_h = lse_r[rows, :]
                        dd_h = dd_r[rows, :]
                        sh = sc[r, :] + bias
                        p = jnp.exp(jnp.concatenate([sh[:, 0:LANES] - lse_h, sh[:, LANES:KWIN] - lse_h], axis=1))
                        dph = dp[r, :]
                        ds = p * jnp.concatenate([dph[:, 0:LANES] - dd_h, dph[:, LANES:KWIN] - dd_h], axis=1)
                        p_buf[slot, r, :] = p.astype(BF16)
                        ds_buf[slot, r, :] = ds.astype(BF16)

                def grads(b, slot):
                    base = _block_base(b)
                    rows = pl.ds(base, QBLK)
                    win = pl.ds(base, KWIN)
                    p = p_buf[slot]
                    ds = ds_buf[slot]
                    dq2 = jnp.dot(ds, k_rm[win, :], preferred_element_type=F32)
                    dq_rm[rows, :] = jnp.where(low, dq2[0:QBLK, :], dq2[QBLK:2 * QBLK, :])
                    dk_rm[win, :] += lax.dot_general(ds, stacked(q_rm, rows), tn_dims, preferred_element_type=F32)
                    dv_rm[win, :] += lax.dot_general(p, stacked(do_rm, rows), tn_dims, preferred_element_type=F32)

                scores(0, 0)

                def pair(i, carry):
                    b = 2 * i
                    grads(b, 0)
                    scores(b + 1, 1)
                    grads(b + 1, 1)
                    scores(b + 2, 0)
                    return carry

                lax.fori_loop(0, n_blk // 2 - 1, pair, 0)
                grads(n_blk - 2, 0)
                scores(n_blk - 1, 1)
                grads(n_blk - 1, 1)
                _from_residue_major(dq_ref, dq_rm, s, d)
                _from_residue_major(dk_ref, dk_rm, s, d, src_off=HALF_SPAN)
                _from_residue_major(dv_ref, dv_rm, s, d, src_off=HALF_SPAN)

    qk_spec = pl.BlockSpec((s, LANES), lambda hp, g: (0, g * n_pairs + hp))
    v_spec = pl.BlockSpec((s, LANES), lambda hp, g: (0, v_col0 + g * n_pairs + hp))
    o_spec = pl.BlockSpec((s, LANES), lambda hp, g: (0, hp))
    width = qn.shape[1]
    f32buf = pltpu.VMEM((s, LANES), F32)
    f32pad = pltpu.VMEM((spad, LANES), F32)
    return _pallas(
        body, name=name, grid=(n_pairs, len(DILATIONS)),
        in_specs=[qk_spec, qk_spec, v_spec, o_spec, o_spec, o_spec,
                  pl.BlockSpec((LANES, LANES), lambda hp, g: (0, 0))],
        out_specs=[qk_spec, qk_spec, qk_spec],
        out_shape=[jax.ShapeDtypeStruct((s, width), F32)] * 3,
        operands=[qn, kn, proj, dattn, attn, lse, bd],
        scratch_shapes=[pltpu.VMEM((s, LANES), BF16), pltpu.VMEM((spad, LANES), BF16),
                        pltpu.VMEM((spad, LANES), BF16), pltpu.VMEM((s, LANES), BF16),
                        f32buf, f32buf, f32buf, f32buf, f32buf, f32pad, f32pad,
                        f32buf, f32buf, f32buf, f32buf, pltpu.VMEM((QBLK, KWIN), F32),
                        pltpu.VMEM((2, 2 * QBLK, KWIN), BF16), pltpu.VMEM((2, 2 * QBLK, KWIN), BF16)],
        rider=rider)


CONV_PAD = 16


def _conv_fwd(proj, conv_w, conv_b, col0, name, rider=None):
    s = proj.shape[0]
    ch = conv_w.shape[1]
    nblk = ch // LANES
    a0 = col0 // LANES
    tr = 256
    shift = CONV_PAD - (CONV_WIDTH - 1) // 2

    def body(a_ref, b_ref, w_ref, bias_ref, u0_ref, uc_ref, pad):
        z = jnp.zeros((CONV_PAD, LANES), F32)
        pad[0:CONV_PAD, :] = z
        pad[s + CONV_PAD: s + 2 * CONV_PAD, :] = z

        def glu(rows):
            u0 = a_ref[rows, :] * jax.nn.sigmoid(b_ref[rows, :])
            u0_ref[rows, :] = u0
            pad[pl.ds(rows.start + CONV_PAD, rows.size), :] = u0

        _row_chunks(s, glu)
        for t in range(0, s, tr):
            acc = jnp.broadcast_to(bias_ref[...], (tr, LANES))
            for k in range(CONV_WIDTH):
                acc = acc + w_ref[k:k + 1, :] * pad[t + k + shift: t + k + shift + tr, :]
            uc_ref[t:t + tr, :] = acc

    return _pallas(
        body, name=name, grid=(nblk,),
        in_specs=[pl.BlockSpec((s, LANES), lambda c: (0, a0 + c)),
                  pl.BlockSpec((s, LANES), lambda c: (0, a0 + nblk + c)),
                  pl.BlockSpec((CONV_WIDTH, LANES), lambda c: (0, c)),
                  pl.BlockSpec((1, LANES), lambda c: (0, c))],
        out_specs=[pl.BlockSpec((s, LANES), lambda c: (0, c))] * 2,
        out_shape=[jax.ShapeDtypeStruct((s, ch), F32)] * 2, operands=[proj, proj, conv_w, conv_b],
        scratch_shapes=[pltpu.VMEM((s + 2 * CONV_PAD, LANES), F32)], rider=rider)


def _ln_silu_fwd(uc, ln_w, ln_b, name):
    s, ch = uc.shape
    tm = 256

    def body(u_ref, w_ref, b_ref, o_ref):
        u = u_ref[...]
        mu = jnp.mean(u, axis=-1, keepdims=True)
        xc = u - mu
        rstd = lax.rsqrt(jnp.mean(xc * xc, axis=-1, keepdims=True) + EPS)
        z = xc * rstd * w_ref[...] + b_ref[...]
        o_ref[...] = (z * jax.nn.sigmoid(z)).astype(BF16)

    row = pl.BlockSpec((tm, ch), lambda i: (i, 0))
    vec = pl.BlockSpec((1, ch), lambda i: (0, 0))
    return pl.pallas_call(
        body, name=name, grid=(s // tm,), in_specs=[row, vec, vec], out_specs=row,
        out_shape=jax.ShapeDtypeStruct((s, ch), BF16), compiler_params=_params(),
    )(uc, ln_w, ln_b)


def _ln_silu_bwd(du3, uc, ln_w, ln_b, name):
    s, ch = uc.shape
    tm = 256

    def body(d_ref, u_ref, w_ref, b_ref, du_ref, dw_ref, db_ref):
        u = u_ref[...]
        mu = jnp.mean(u, axis=-1, keepdims=True)
        xc = u - mu
        rstd = lax.rsqrt(jnp.mean(xc * xc, axis=-1, keepdims=True) + EPS)
        xhat = xc * rstd
        z = xhat * w_ref[...] + b_ref[...]
        sg = jax.nn.sigmoid(z)
        dz = d_ref[...] * (sg * (1.0 + z * (1.0 - sg)))
        dxh = dz * w_ref[...]
        du_ref[...] = rstd * (dxh - jnp.mean(dxh, axis=-1, keepdims=True)
                              - xhat * jnp.mean(dxh * xhat, axis=-1, keepdims=True))
        pw = jnp.sum(dz * xhat, axis=0, keepdims=True)
        pb = jnp.sum(dz, axis=0, keepdims=True)
        first = pl.program_id(0) == 0

        @pl.when(first)
        def _():
            dw_ref[...] = pw
            db_ref[...] = pb

        @pl.when(jnp.logical_not(first))
        def _():
            dw_ref[...] += pw
            db_ref[...] += pb

    row = pl.BlockSpec((tm, ch), lambda i: (i, 0))
    vec = pl.BlockSpec((1, ch), lambda i: (0, 0))
    return pl.pallas_call(
        body, name=name, grid=(s // tm,), in_specs=[row, row, vec, vec], out_specs=[row, vec, vec],
        out_shape=[jax.ShapeDtypeStruct((s, ch), F32), jax.ShapeDtypeStruct((1, ch), F32),
                   jax.ShapeDtypeStruct((1, ch), F32)],
        compiler_params=_params(),
    )(du3, uc, ln_w, ln_b)


def _conv_bwd(duc, u0, proj, conv_w, col0, name, rider=None):
    s = proj.shape[0]
    ch = conv_w.shape[1]
    nblk = ch // LANES
    a0 = col0 // LANES
    tr = 256
    half = (CONV_WIDTH - 1) // 2
    shift = CONV_PAD - half

    def body(duc_ref, u0_ref, a_ref, b_ref, w_ref, da_ref, db_ref, dw_ref, dbias_ref, pad_d, pad_u):
        z = jnp.zeros((CONV_PAD, LANES), F32)
        for buf in (pad_d, pad_u):
            buf[0:CONV_PAD, :] = z
            buf[s + CONV_PAD: s + 2 * CONV_PAD, :] = z

        def fill(rows):
            dst = pl.ds(rows.start + CONV_PAD, rows.size)
            pad_d[dst, :] = duc_ref[rows, :]
            pad_u[dst, :] = u0_ref[rows, :]

        _row_chunks(s, fill)
        dw_acc = [jnp.zeros((8, LANES), F32) for _ in range(CONV_WIDTH)]
        dbias_acc = jnp.zeros((8, LANES), F32)
        for t in range(0, s, tr):
            d_t = duc_ref[t:t + tr, :]
            dbias_acc = dbias_acc + jnp.sum(d_t.reshape(tr // 8, 8, LANES), axis=0)
            du0 = jnp.zeros((tr, LANES), F32)
            for k in range(CONV_WIDTH):
                du0 = du0 + w_ref[k:k + 1, :] * pad_d[t - k + half + CONV_PAD: t - k + half + CONV_PAD + tr, :]
                prod = d_t * pad_u[t + k + shift: t + k + shift + tr, :]
                dw_acc[k] = dw_acc[k] + jnp.sum(prod.reshape(tr // 8, 8, LANES), axis=0)
            av = a_ref[t:t + tr, :]
            sg = jax.nn.sigmoid(b_ref[t:t + tr, :])
            da_ref[t:t + tr, :] = (du0 * sg).astype(BF16)
            db_ref[t:t + tr, :] = (du0 * av * sg * (1.0 - sg)).astype(BF16)
        for k in range(CONV_WIDTH):
            dw_ref[k:k + 1, :] = jnp.sum(dw_acc[k], axis=0, keepdims=True)
        dbias_ref[...] = jnp.sum(dbias_acc, axis=0, keepdims=True)

    col = lambda off: pl.BlockSpec((s, LANES), lambda c: (0, off + c))
    return _pallas(
        body, name=name, grid=(nblk,),
        in_specs=[col(0), col(0), col(a0), col(a0 + nblk),
                  pl.BlockSpec((CONV_WIDTH, LANES), lambda c: (0, c))],
        out_specs=[col(0), col(0), pl.BlockSpec((CONV_WIDTH, LANES), lambda c: (0, c)),
                   pl.BlockSpec((1, LANES), lambda c: (0, c))],
        out_shape=[jax.ShapeDtypeStruct((s, ch), BF16)] * 2
        + [jax.ShapeDtypeStruct((CONV_WIDTH, ch), F32), jax.ShapeDtypeStruct((1, ch), F32)],
        operands=[duc, u0, proj, proj, conv_w],
        scratch_shapes=[pltpu.VMEM((s + 2 * CONV_PAD, LANES), F32)] * 2, rider=rider)


GATE_BLK = 512


def _gate_fwd(proj, bg, y_a, y_b, col0, name):
    s, d = y_a.shape
    tm = 256
    g0 = col0 // GATE_BLK
    nb = d // GATE_BLK

    def body(ga_ref, gb_ref, ba_ref, bb_ref, ya_ref, yb_ref, o_ref):
        g_a = jax.nn.sigmoid(ga_ref[...] + ba_ref[...])
        g_b = jax.nn.sigmoid(gb_ref[...] + bb_ref[...])
        o_ref[...] = (g_a * ya_ref[...] + g_b * yb_ref[...]).astype(BF16)

    act = pl.BlockSpec((tm, GATE_BLK), lambda i, j: (i, j))
    return pl.pallas_call(
        body, name=name, grid=(s // tm, nb),
        in_specs=[pl.BlockSpec((tm, GATE_BLK), lambda i, j: (i, g0 + j)),
                  pl.BlockSpec((tm, GATE_BLK), lambda i, j: (i, g0 + nb + j)),
                  pl.BlockSpec((None, 1, GATE_BLK), lambda i, j: (0, 0, j)),
                  pl.BlockSpec((None, 1, GATE_BLK), lambda i, j: (1, 0, j)), act, act],
        out_specs=act, out_shape=jax.ShapeDtypeStruct((s, d), BF16), compiler_params=_params(),
    )(proj, proj, bg, bg, y_a, y_b)


def _gate_bwd(d_mixed, proj, bg, y_a, y_b, col0, name, rider=None):
    s, d = y_a.shape
    tm = 256
    half = d // 2
    assert col0 % half == 0
    c0 = col0 // half

    def body(dm_ref, a0_ref, a1_ref, b0_ref, b1_ref, bias_ref, ya_ref, yb_ref, dgl_ref, dya_ref, dyb_ref, db_ref):
        dm = dm_ref[...]
        parts = []
        for br, (lo_ref, hi_ref, y_ref, dy_ref) in enumerate(((a0_ref, a1_ref, ya_ref, dya_ref),
                                                              (b0_ref, b1_ref, yb_ref, dyb_ref))):
            logits = jnp.concatenate([lo_ref[...], hi_ref[...]], axis=1)
            gate = jax.nn.sigmoid(logits + bias_ref[br])
            dy_ref[...] = (dm * gate).astype(BF16)
            dgl = dm * y_ref[...] * gate * (1.0 - gate)
            dgl_ref[:, br * d:(br + 1) * d] = dgl.astype(BF16)
            parts.append(jnp.sum(dgl, axis=0, keepdims=True))
        part = jnp.concatenate(parts, axis=0)
        first = pl.program_id(0) == 0

        @pl.when(first)
        def _():
            db_ref[...] = part

        @pl.when(jnp.logical_not(first))
        def _():
            db_ref[...] += part

    row = pl.BlockSpec((tm, d), lambda i: (i, 0))
    logit_blk = lambda k: pl.BlockSpec((tm, half), functools.partial(lambda i, k: (i, c0 + k), k=k))
    return _pallas(
        body, name=name, grid=(s // tm,),
        in_specs=[row, logit_blk(0), logit_blk(1), logit_blk(2), logit_blk(3),
                  pl.BlockSpec((2, 1, d), lambda i: (0, 0, 0)), row, row],
        out_specs=[pl.BlockSpec((tm, 2 * d), lambda i: (i, 0)), row, row, pl.BlockSpec((2, d), lambda i: (0, 0))],
        out_shape=[jax.ShapeDtypeStruct((s, 2 * d), BF16), jax.ShapeDtypeStruct((s, d), BF16),
                   jax.ShapeDtypeStruct((s, d), BF16), jax.ShapeDtypeStruct((2, d), F32)],
        operands=[d_mixed, proj, proj, proj, proj, bg, y_a, y_b], rider=rider)


def _ffn_in_swiglu(h2, w_blocked, name):
    s, k = h2.shape
    nblk, _, tn = w_blocked.shape
    ff = nblk // 2 * tn
    tm = 512

    def body(a_ref, wg_ref, wu_ref, g_ref, u_ref, act_ref):
        a = a_ref[...]
        gt = jnp.dot(a, wg_ref[...], preferred_element_type=F32)
        up = jnp.dot(a, wu_ref[...], preferred_element_type=F32)
        g_ref[...] = gt
        u_ref[...] = up
        act_ref[...] = (gt * jax.nn.sigmoid(gt) * up).astype(BF16)

    out = pl.BlockSpec((tm, tn), lambda i, j: (i, j))
    return pl.pallas_call(
        body, name=name, grid=(s // tm, nblk // 2),
        in_specs=[pl.BlockSpec((tm, k), lambda i, j: (i, 0)),
                  pl.BlockSpec((None, k, tn), lambda i, j: (j, 0, 0)),
                  pl.BlockSpec((None, k, tn), lambda i, j: (nblk // 2 + j, 0, 0))],
        out_specs=[out, out, out],
        out_shape=[jax.ShapeDtypeStruct((s, ff), F32), jax.ShapeDtypeStruct((s, ff), F32),
                   jax.ShapeDtypeStruct((s, ff), BF16)],
        compiler_params=_params(),
    )(h2, w_blocked, w_blocked)


def _swiglu_bwd(gate, up, d_act, name, rider=None):
    s, ff = gate.shape
    tm = 256

    def body(g_ref, u_ref, d_ref, o_ref):
        gt = g_ref[...]
        sg = jax.nn.sigmoid(gt)
        dv = d_ref[...]
        o_ref[:, 0:ff] = (dv * u_ref[...] * (sg * (1.0 + gt * (1.0 - sg)))).astype(BF16)
        o_ref[:, ff:2 * ff] = (dv * gt * sg).astype(BF16)

    row = pl.BlockSpec((tm, ff), lambda i: (i, 0))
    return _pallas(
        body, name=name, grid=(s // tm,), in_specs=[row, row, row],
        out_specs=pl.BlockSpec((tm, 2 * ff), lambda i: (i, 0)),
        out_shape=jax.ShapeDtypeStruct((s, 2 * ff), BF16), operands=[gate, up, d_act], rider=rider)


def _out_proj_rmsnorm(mixed, w_out, x, norm_w, name):
    s, k = mixed.shape
    d = w_out.shape[1]
    tm = 512

    def body(a_ref, w_ref, x_ref, nw_ref, x1_ref, h2_ref):
        x1 = x_ref[...] + jnp.dot(a_ref[...], w_ref[...], preferred_element_type=F32)
        x1_ref[...] = x1
        rstd = lax.rsqrt(jnp.mean(x1 * x1, axis=-1, keepdims=True) + EPS)
        h2_ref[...] = (x1 * rstd * nw_ref[...]).astype(BF16)

    row = pl.BlockSpec((tm, d), lambda i: (i, 0))
    return pl.pallas_call(
        body, name=name, grid=(s // tm,),
        in_specs=[pl.BlockSpec((tm, k), lambda i: (i, 0)), pl.BlockSpec((k, d), lambda i: (0, 0)), row,
                  pl.BlockSpec((1, d), lambda i: (0, 0))],
        out_specs=[row, row],
        out_shape=[jax.ShapeDtypeStruct((s, d), F32), jax.ShapeDtypeStruct((s, d), BF16)],
        compiler_params=_params(),
    )(mixed, w_out, x, norm_w)


def _ffn_out_loss(act, w_ffn_out, x1, target, name):
    s, k = act.shape
    d = w_ffn_out.shape[1]
    tm = 512

    def body(a_ref, w_ref, x1_ref, t_ref, dy_ref, dyb_ref, loss_ref, acc):
        y = x1_ref[...] + jnp.dot(a_ref[...], w_ref[...], preferred_element_type=F32)
        diff = y - t_ref[...]
        dy = diff * (1.0 / d)
        dy_ref[...] = dy
        dyb_ref[...] = dy.astype(BF16)
        part = jnp.sum((diff * diff).reshape(tm // 8, 8, d), axis=0)
        i = pl.program_id(0)

        @pl.when(i == 0)
        def _():
            acc[...] = part

        @pl.when(i > 0)
        def _():
            acc[...] += part

        @pl.when(i == pl.num_programs(0) - 1)
        def _():
            loss_ref[...] = (0.5 / d) * jnp.sum(jnp.sum(acc[...], axis=1, keepdims=True), axis=0, keepdims=True)

    row = pl.BlockSpec((tm, d), lambda i: (i, 0))
    return pl.pallas_call(
        body, name=name, grid=(s // tm,),
        in_specs=[pl.BlockSpec((tm, k), lambda i: (i, 0)), pl.BlockSpec((k, d), lambda i: (0, 0)), row, row],
        out_specs=[row, row, pl.BlockSpec((1, 1), lambda i: (0, 0))],
        out_shape=[jax.ShapeDtypeStruct((s, d), F32), jax.ShapeDtypeStruct((s, d), BF16),
                   jax.ShapeDtypeStruct((1, 1), F32)],
        scratch_shapes=[pltpu.VMEM((8, d), F32)], compiler_params=_params(),
    )(act, w_ffn_out, x1, target)


LATE_GATHER = ("w_o_attn", "w_pw_conv", "w_out", "w_ffn_in", "w_ffn_out")
EARLY_REDUCE = LATE_GATHER


def _blocks_by_half(g):
    if g.ndim == 2:
        g = g.reshape(N_CHIPS, g.shape[0] // N_CHIPS, g.shape[1])
    return g.reshape(N_CHIPS, 2, g.shape[1] // 2, g.shape[2])


def _forward_backward(x, pos_col, target, wts, late_bufs, pos_arr):
    wts = dict(wts)
    consts = _rope_consts()
    bd = consts[3]
    qw2 = jnp.tile(wts["q_norm_w"], (1, LANES // HEAD_DIM))
    kw2 = jnp.tile(wts["k_norm_w"], (1, LANES // HEAD_DIM))
    qkv_w = 3 * N_SLOT_HEADS * HEAD_DIM
    conv_col0 = 3 * qkv_w
    ch = wts["conv_w"].shape[1]
    gate_col0 = conv_col0 + 2 * ch

    h = _rmsnorm_fwd(x, wts["norm1_w"], "rms1_fwd")
    late = dict(zip(LATE_GATHER, late_bufs))
    quarter = late["w_ffn_in"].shape[2] // 4
    proj, (late["w_o_attn"], late["w_pw_conv"], late["w_out"], late["w_ffn_in"]) = _matmul(
        h, wts["w_in"], mode="nn", tm=512, tn=1920, tk=1024, out_dtype=F32, name="mm_proj", b_blocked=True,
        rider=_gather_ici_rider([late["w_o_attn"], late["w_pw_conv"], late["w_out"], late["w_ffn_in"]], [],
                                row_ranges=[None, None, None, (0, quarter)]))
    (qn, kn), (late["w_ffn_in"],) = _qk_fwd(
        proj, pos_col, qw2, kw2, consts, "qk_fwd",
        rider=_gather_ici_rider([late["w_ffn_in"]], [], row_ranges=[(quarter, 2 * quarter)]))
    (attn, lse, attn_b), (late["w_ffn_in"], late["w_ffn_out"]) = _attn_fwd(
        qn, kn, proj, "attn_fwd",
        rider=_gather_ici_rider([late["w_ffn_in"], late["w_ffn_out"]], [],
                                row_ranges=[(3 * quarter, quarter), None]))
    (u0, uc), late_bufs = _conv_fwd(proj, wts["conv_w"], wts["conv_b"], conv_col0, "conv_fwd",
                                    rider=_gather_forward_rider([late[n] for n in LATE_GATHER]))
    for n, buf in zip(LATE_GATHER, late_bufs):
        full = buf.reshape(N_CHIPS, -1, buf.shape[3])
        wts[n] = full.reshape(-1, full.shape[2]) if n in ROW_SHARDED else full
    y_a = _matmul(attn_b, wts["w_o_attn"], mode="nn", tm=1024, tn=256, tk=512, out_dtype=F32, name="mm_ya",
                  b_blocked=True)
    u3 = _ln_silu_fwd(uc, wts["conv_ln_w"], wts["conv_ln_b"], "ln_fwd")
    y_b = _matmul(u3, wts["w_pw_conv"], mode="nn", tm=1024, tn=256, tk=512, out_dtype=F32, name="mm_yb",
                  b_blocked=True)
    mixed = _gate_fwd(proj, wts["b_gate"], y_a, y_b, gate_col0, "gate_fwd")
    x1, h2 = _out_proj_rmsnorm(mixed, wts["w_out"], x, wts["norm2_w"], "mm_x1_rms2")
    gate, up, act = _ffn_in_swiglu(h2, wts["w_ffn_in"], "mm_gu_swiglu")
    dy, dy_b16, loss = _ffn_out_loss(act, wts["w_ffn_out"], x1, target, "mm_x2_loss")

    g = {}
    by_chip = {}

    def pair_add(n, blocks, received):
        return _add_own_half(blocks, received, pos_arr, f"grads_pair_add_{n}")

    d_act = _matmul(dy_b16, wts["w_ffn_out"], mode="nt", tm=512, tn=1408, tk=1024, out_dtype=F32, name="mm_dact")
    g_ffn_out = _blocks_by_half(
        _matmul(act, dy_b16, mode="tn", tm=1408, tn=1024, tk=2048, out_dtype=F32, name="mm_dwffnout"))
    dgu, (received,) = _swiglu_bwd(gate, up, d_act, "swiglu_bwd",
                                   rider=_pair_exchange_rider([g_ffn_out], halved=True))
    to_send, own = pair_add("w_ffn_out", g_ffn_out, received)
    dh2, (by_chip["w_ffn_out"],) = _matmul(
        dgu, wts["w_ffn_in"], mode="nt", tm=1024, tn=1024, tk=1408, out_dtype=F32, name="mm_dh2", b_blocked=True,
        rider=_chip_exchange_rider([to_send], [own]))
    g_ffn_in = _blocks_by_half(_matmul(h2, dgu, mode="tn", tm=512, tn=1408, tk=2048, out_dtype=F32,
                                       name="mm_dwffnin", out_blocked=N_CHIPS))
    dx1, dx1_b16, g["norm2_w"] = _rmsnorm_bwd(dh2, x1, wts["norm2_w"], dy, "rms2_bwd")
    d_mixed = _matmul(dx1_b16, wts["w_out"], mode="nt", tm=512, tn=1024, tk=1024, out_dtype=F32, name="mm_dmixed")
    g["w_out"] = _matmul(mixed, dx1_b16, mode="tn", tm=512, tn=1024, tk=2048, out_dtype=F32, name="mm_dwout")
    (dgl, dy_a, dy_b, g["b_gate"]), (received,) = _gate_bwd(
        d_mixed, proj, wts["b_gate"], y_a, y_b, gate_col0, "gate_bwd",
        rider=_pair_exchange_rider([g_ffn_in], halved=True))
    ffn_in_to_send, ffn_in_own = pair_add("w_ffn_in", g_ffn_in, received)
    dattn = _matmul(dy_a, wts["w_o_attn"], mode="nt", tm=1024, tn=512, tk=256, out_dtype=F32, name="mm_dattn",
                    b_blocked=True)
    g["w_o_attn"] = _matmul(attn_b, dy_a, mode="tn", tm=512, tn=256, tk=2048, out_dtype=F32, name="mm_dwo",
                            out_blocked=N_CHIPS)
    du3 = _matmul(dy_b, wts["w_pw_conv"], mode="nt", tm=1024, tn=512, tk=256, out_dtype=F32, name="mm_du3",
                  b_blocked=True)
    g["w_pw_conv"] = _matmul(u3, dy_b, mode="tn", tm=512, tn=256, tk=2048, out_dtype=F32, name="mm_dwpw",
                             out_blocked=N_CHIPS)
    duc, g["conv_ln_w"], g["conv_ln_b"] = _ln_silu_bwd(du3, uc, wts["conv_ln_w"], wts["conv_ln_b"], "ln_bwd")

    small3 = ("w_out", "w_o_attn", "w_pw_conv")
    g_small3 = [_blocks_by_half(g.pop(n)) for n in small3]
    (da, db, g["conv_w"], g["conv_b"]), received = _conv_bwd(
        duc, u0, proj, wts["conv_w"], conv_col0, "conv_bwd", rider=_pair_exchange_rider(g_small3, halved=True))
    sums3 = [pair_add(n, gb, rv) for n, gb, rv in zip(small3, g_small3, received)]
    (dqn, dkn, dv), (by_chip["w_ffn_in"],) = _attn_bwd(
        qn, kn, proj, dattn, attn, lse, bd, "attn_bwd",
        rider=_chip_exchange_rider([ffn_in_to_send], [ffn_in_own]))
    (dproj, dqw, dkw), exchanged3 = _qk_bwd(
        dqn, dkn, dv, da, db, dgl, proj, pos_col, qw2, kw2, consts, "qk_bwd",
        rider=_chip_exchange_rider([s[0] for s in sums3], [s[1] for s in sums3]))
    by_chip.update(zip(small3, exchanged3))
    halves = [_sum_chips(by_chip[n], pos_arr, f"grads_chip_sum_{n}") for n in EARLY_REDUCE]
    g["q_norm_w"] = dqw[:, :HEAD_DIM]
    g["k_norm_w"] = dkw[:, :HEAD_DIM]

    c = pos_arr[0]
    rh = h.shape[1] // 2
    h_sibling = lax.dynamic_slice_in_dim(h, (1 - c) * rh, rh, axis=1)
    h_own = lax.dynamic_slice_in_dim(h, c * rh, rh, axis=1)
    g_sibling, shards = _matmul(h_sibling, dproj, mode="tn", tm=rh, tn=1920, tk=2048, out_dtype=F32,
                                name="mm_dwin_sibling", out_blocked=N_CHIPS, rider=_pair_gather_rider(halves))
    reduced = dict(zip(EARLY_REDUCE, shards))
    g_own, from_sibling = _matmul(h_own, dproj, mode="tn", tm=rh, tn=1920, tk=2048, out_dtype=F32,
                                  name="mm_dwin_own", out_blocked=N_CHIPS,
                                  rider=_pair_exchange_rider([g_sibling], halved=False))
    to_send, own = _add_own_half(g_own, from_sibling[0], pos_arr, "grads_pair_add_w_in")
    *in_flight, token = _chip_exchange_start(to_send, own, "grads_w_in_exchange_start")
    in_flight = tuple(in_flight) + (token,)
    dh = _matmul(dproj, wts["w_in"], mode="nt", tm=1024, tn=1024, tk=1920, out_dtype=F32, name="mm_dh",
                 b_blocked=True, after=[token])
    grad_x, _, g["norm1_w"] = _rmsnorm_bwd(dh, x, wts["norm1_w"], dx1, "rms1_bwd")
    return loss, grad_x, g, reduced, in_flight


def _mesh_pos():
    return lax.axis_index("x"), lax.axis_index("y"), lax.axis_index("c")


def _other_chips(x, y):
    return [(1 - x, y), (x, 1 - y), (1 - x, 1 - y)]


def _cast_into_slot(shard, chip_arr, dtype, name):
    r, c = shard.shape
    tr = r // 2 if r % 32 == 0 else r

    def body(chip_ref, s_ref, o_ref):
        del chip_ref
        o_ref[...] = s_ref[...].astype(dtype)

    return pl.pallas_call(
        body, name=name,
        grid_spec=pltpu.PrefetchScalarGridSpec(
            num_scalar_prefetch=1, grid=(r // tr,),
            in_specs=[pl.BlockSpec((tr, c), lambda i, chip_ref: (i, 0))],
            out_specs=pl.BlockSpec((None, tr, c), lambda i, chip_ref: (chip_ref[0], i, 0))),
        out_shape=jax.ShapeDtypeStruct((N_CHIPS, r, c), dtype), compiler_params=_params(),
    )(chip_arr, shard)


def _allgather_inplace(big, small, name):
    nb, ns = len(big), len(small)
    n = nb + ns

    def body(*refs):
        bufs = refs[n:2 * n]
        send_sems, recv_sems, fsend_sems, frecv_sems = refs[2 * n:]
        x, y, c = _mesh_pos()
        me = 2 * x + y
        chips = _other_chips(x, y)

        def part(a, slot, half):
            return bufs[a].at[slot, half] if a < nb else bufs[a].at[slot]

        sends = []
        for a in range(n):
            for k, (px, py) in enumerate(chips):
                cp = pltpu.make_async_remote_copy(
                    src_ref=part(a, me, c), dst_ref=part(a, me, c), send_sem=send_sems.at[a, k],
                    recv_sem=recv_sems.at[a, k], device_id=(px, py, c), device_id_type=MESH)
                cp.start()
                sends.append(cp)
        for a in range(n):
            for k, (px, py) in enumerate(chips):
                slot = 2 * px + py
                pltpu.make_async_remote_copy(
                    src_ref=part(a, slot, c), dst_ref=part(a, slot, c), send_sem=send_sems.at[a, k],
                    recv_sem=recv_sems.at[a, k], device_id=(px, py, c), device_id_type=MESH).wait_recv()
                if a < nb:
                    fwd = pltpu.make_async_remote_copy(
                        src_ref=part(a, slot, c), dst_ref=part(a, slot, c), send_sem=fsend_sems.at[a, k],
                        recv_sem=frecv_sems.at[a, k], device_id=(x, y, 1 - c), device_id_type=MESH)
                    fwd.start()
                    sends.append(fwd)
        for a in range(nb):
            for k, (px, py) in enumerate(chips):
                slot = 2 * px + py
                pltpu.make_async_remote_copy(
                    src_ref=part(a, slot, 1 - c), dst_ref=part(a, slot, 1 - c), send_sem=fsend_sems.at[a, k],
                    recv_sem=frecv_sems.at[a, k], device_id=(x, y, 1 - c), device_id_type=MESH).wait_recv()
        for cp in sends:
            cp.wait_send()

    ops = list(big) + list(small)
    return pl.pallas_call(
        body, name=name, in_specs=[ANY] * n, out_specs=[ANY] * n,
        out_shape=[jax.ShapeDtypeStruct(o.shape, o.dtype) for o in ops],
        input_output_aliases={i: i for i in range(n)},
        scratch_shapes=[pltpu.SemaphoreType.DMA((n, 3)), pltpu.SemaphoreType.DMA((n, 3)),
                        pltpu.SemaphoreType.DMA((nb, 3)), pltpu.SemaphoreType.DMA((nb, 3))],
    )(*ops)


def _comm_call(rider, name):
    def body():
        pass

    return _pallas(body, name=name, grid=(1,), in_specs=[], out_specs=[], out_shape=[], operands=[],
                   rider=rider)[1]


def _gather_ici_rider(big, small, row_ranges=None):
    nb = len(big)
    n = nb + len(small)
    row_ranges = row_ranges or [None] * nb

    def copies(bufs, sems):
        x, y, c = _mesh_pos()
        me = 2 * x + y

        def part(a, slot):
            if a >= nb:
                return bufs[a].at[slot]
            if row_ranges[a] is None:
                return bufs[a].at[slot, c]
            return bufs[a].at[slot, c, pl.ds(*row_ranges[a])]
        out = []
        for a in range(n):
            for k, (px, py) in enumerate(_other_chips(x, y)):
                send = functools.partial(
                    pltpu.make_async_remote_copy,
                    src_ref=part(a, me), dst_ref=part(a, me), send_sem=sems[0].at[a, k],
                    recv_sem=sems[1].at[a, k], device_id=(px, py, c), device_id_type=MESH)
                recv = functools.partial(
                    pltpu.make_async_remote_copy,
                    src_ref=part(a, 2 * px + py), dst_ref=part(a, 2 * px + py), send_sem=sems[0].at[a, k],
                    recv_sem=sems[1].at[a, k], device_id=(px, py, c), device_id_type=MESH)
                out.append((send, recv))
        return out

    def start(r_in, r_out, sems):
        for send, _ in copies(r_out, sems):
            send().start()

    def wait(r_in, r_out, sems):
        cps = copies(r_out, sems)
        for _, recv in cps:
            recv().wait_recv()
        for send, _ in cps:
            send().wait_send()

    ops = list(big) + list(small)
    return _Rider(ops, [jax.ShapeDtypeStruct(o.shape, o.dtype) for o in ops], {i: i for i in range(n)},
                  [pltpu.SemaphoreType.DMA((n, 3)), pltpu.SemaphoreType.DMA((n, 3))], start, wait)


def _gather_forward_rider(big):
    n = len(big)

    def copies(bufs, sems):
        x, y, c = _mesh_pos()
        out = []
        for a in range(n):
            for k, (px, py) in enumerate(_other_chips(x, y)):
                slot = 2 * px + py
                send = functools.partial(
                    pltpu.make_async_remote_copy,
                    src_ref=bufs[a].at[slot, c], dst_ref=bufs[a].at[slot, c], send_sem=sems[0].at[a, k],
                    recv_sem=sems[1].at[a, k], device_id=(x, y, 1 - c), device_id_type=MESH)
                recv = functools.partial(
                    pltpu.make_async_remote_copy,
                    src_ref=bufs[a].at[slot, 1 - c], dst_ref=bufs[a].at[slot, 1 - c], send_sem=sems[0].at[a, k],
                    recv_sem=sems[1].at[a, k], device_id=(x, y, 1 - c), device_id_type=MESH)
                out.append((send, recv))
        return out

    def start(r_in, r_out, sems):
        for send, _ in copies(r_out, sems):
            send().start()

    def wait(r_in, r_out, sems):
        cps = copies(r_out, sems)
        for _, recv in cps:
            recv().wait_recv()
        for send, _ in cps:
            send().wait_send()

    return _Rider(big, [jax.ShapeDtypeStruct(o.shape, o.dtype) for o in big], {i: i for i in range(n)},
                  [pltpu.SemaphoreType.DMA((n, 3)), pltpu.SemaphoreType.DMA((n, 3))], start, wait)


def _pair_exchange_rider(gs, halved):
    n = len(gs)

    def copies(r_in, r_out, sems):
        x, y, c = _mesh_pos()
        return [pltpu.make_async_remote_copy(
            src_ref=r_in[a].at[:, 1 - c] if halved else r_in[a], dst_ref=r_out[a], send_sem=sems[0].at[a],
            recv_sem=sems[1].at[a], device_id=(x, y, 1 - c), device_id_type=MESH) for a in range(n)]

    def start(r_in, r_out, sems):
        for cp in copies(r_in, r_out, sems):
            cp.start()

    def wait(r_in, r_out, sems):
        for cp in copies(r_in, r_out, sems):
            cp.wait()

    return _Rider(gs, [jax.ShapeDtypeStruct((g.shape[0],) + g.shape[-2:], g.dtype) for g in gs], {},
                  [pltpu.SemaphoreType.DMA((n,)), pltpu.SemaphoreType.DMA((n,))], start, wait)


def _chip_exchange_rider(to_send, by_chip, row_range=None):
    n = len(to_send)

    def copies(r_in, r_out, sems):
        x, y, c = _mesh_pos()
        me = 2 * x + y
        rows = (lambda ref: ref) if row_range is None else (lambda ref: ref.at[pl.ds(*row_range)])
        out = []
        for a in range(n):
            for k, (px, py) in enumerate(_other_chips(x, y)):
                send = functools.partial(
                    pltpu.make_async_remote_copy,
                    src_ref=rows(r_in[a].at[2 * px + py]), dst_ref=rows(r_out[a].at[me]),
                    send_sem=sems[0].at[a, k], recv_sem=sems[1].at[a, k], device_id=(px, py, c),
                    device_id_type=MESH)
                recv = functools.partial(
                    pltpu.make_async_remote_copy,
                    src_ref=rows(r_in[a].at[me]), dst_ref=rows(r_out[a].at[2 * px + py]),
                    send_sem=sems[0].at[a, k], recv_sem=sems[1].at[a, k], device_id=(px, py, c),
                    device_id_type=MESH)
                out.append((send, recv))
        return out

    def start(r_in, r_out, sems):
        for send, _ in copies(r_in, r_out, sems):
            send().start()

    def wait(r_in, r_out, sems):
        cps = copies(r_in, r_out, sems)
        for _, recv in cps:
            recv().wait_recv()
        for send, _ in cps:
            send().wait_send()

    return _Rider(list(to_send) + list(by_chip), [jax.ShapeDtypeStruct(b.shape, b.dtype) for b in by_chip],
                  {n + i: i for i in range(n)},
                  [pltpu.SemaphoreType.DMA((n, 3)), pltpu.SemaphoreType.DMA((n, 3))], start, wait)


HBM = pl.BlockSpec(memory_space=pltpu.HBM)
SEM = pl.BlockSpec(memory_space=pltpu.SEMAPHORE)


def _chip_exchange_start(to_send, by_chip, name):
    def body(send_ref, buf_ref, send_sems, recv_sems, send_thru, buf_thru, token):
        x, y, c = _mesh_pos()
        me = 2 * x + y
        for k, (px, py) in enumerate(_other_chips(x, y)):
            pltpu.make_async_remote_copy(
                src_ref=send_ref.at[2 * px + py], dst_ref=buf_ref.at[me], send_sem=send_sems.at[k],
                recv_sem=recv_sems.at[k], device_id=(px, py, c), device_id_type=MESH).start()
        token[...] = jnp.zeros_like(token)

    return pl.pallas_call(
        body, name=name,
        out_shape=(pltpu.SemaphoreType.DMA((3,)), pltpu.SemaphoreType.DMA((3,)),
                   pltpu.HBM(to_send.shape, to_send.dtype), pltpu.HBM(by_chip.shape, by_chip.dtype),
                   jax.ShapeDtypeStruct((8, LANES), F32)),
        in_specs=(HBM, HBM), out_specs=(SEM, SEM, HBM, HBM, pl.BlockSpec(memory_space=pltpu.VMEM)),
        input_output_aliases={0: 2, 1: 3},
        compiler_params=pltpu.CompilerParams(has_side_effects=pltpu.SideEffectType.DATAFLOW_SIDE_EFFECTING),
    )(pltpu.with_memory_space_constraint(to_send, pltpu.HBM), pltpu.with_memory_space_constraint(by_chip, pltpu.HBM))


def _chip_exchange_wait(send_sems, recv_sems, send_thru, buf_thru, after, name):
    n_after = len(after)

    def body(send_ref, buf_ref, send_sems, recv_sems, *rest):
        x, y, c = _mesh_pos()
        me = 2 * x + y
        for k, (px, py) in enumerate(_other_chips(x, y)):
            cp = pltpu.make_async_remote_copy(
                src_ref=send_ref.at[me], dst_ref=buf_ref.at[2 * px + py], send_sem=send_sems.at[k],
                recv_sem=recv_sems.at[k], device_id=(px, py, c), device_id_type=MESH)
            cp.wait_send()
            cp.wait_recv()

    return pl.pallas_call(
        body, name=name,
        out_shape=(pltpu.HBM(send_thru.shape, send_thru.dtype), pltpu.HBM(buf_thru.shape, buf_thru.dtype)),
        in_specs=(HBM, HBM, SEM, SEM) + (ANY,) * n_after, out_specs=(HBM, HBM), input_output_aliases={0: 0, 1: 1},
        compiler_params=pltpu.CompilerParams(has_side_effects=pltpu.SideEffectType.DATAFLOW_SIDE_EFFECTING),
    )(send_thru, buf_thru, send_sems, recv_sems, *after)[1]


def _pair_gather_rider(bufs):
    n = len(bufs)

    def copies(r_out, sems):
        x, y, c = _mesh_pos()
        out = []
        for a in range(n):
            send = functools.partial(
                    pltpu.make_async_remote_copy,
                src_ref=r_out[a].at[c], dst_ref=r_out[a].at[c], send_sem=sems[0].at[a],
                recv_sem=sems[1].at[a], device_id=(x, y, 1 - c), device_id_type=MESH)
            recv = functools.partial(
                    pltpu.make_async_remote_copy,
                src_ref=r_out[a].at[1 - c], dst_ref=r_out[a].at[1 - c], send_sem=sems[0].at[a],
                recv_sem=sems[1].at[a], device_id=(x, y, 1 - c), device_id_type=MESH)
            out.append((send, recv))
        return out

    def start(r_in, r_out, sems):
        for send, _ in copies(r_out, sems):
            send().start()

    def wait(r_in, r_out, sems):
        cps = copies(r_out, sems)
        for _, recv in cps:
            recv().wait_recv()
        for send, _ in cps:
            send().wait_send()

    return _Rider(bufs, [jax.ShapeDtypeStruct(b.shape, b.dtype) for b in bufs], {i: i for i in range(n)},
                  [pltpu.SemaphoreType.DMA((n,)), pltpu.SemaphoreType.DMA((n,))], start, wait)


def _add_own_half(g, recv, pos_arr, name):
    nb, rh, cols = g.shape[0], g.shape[-2], g.shape[-1]

    def body(pos_ref, g_ref, r_ref, send_ref, own_ref):
        s = (g_ref[...] + r_ref[...]).astype(BF16)
        send_ref[...] = s

        @pl.when(pl.program_id(0) == pos_ref[1])
        def _():
            own_ref[...] = s

    blk = pl.BlockSpec((None, rh, cols), lambda j, pos_ref: (j, 0, 0))
    g_spec = blk if g.ndim == 3 else pl.BlockSpec((None, None, rh, cols),
                                                   lambda j, pos_ref: (j, pos_ref[0], 0, 0))
    shape = jax.ShapeDtypeStruct((nb, rh, cols), BF16)
    return pl.pallas_call(
        body, name=name,
        grid_spec=pltpu.PrefetchScalarGridSpec(
            num_scalar_prefetch=1, grid=(nb,), in_specs=[g_spec, blk],
            out_specs=[blk, pl.BlockSpec((None, rh, cols), lambda j, pos_ref: (pos_ref[1], 0, 0))]),
        out_shape=[shape, shape], compiler_params=_params(),
    )(pos_arr, g, recv)


def _sum_chips(gath, pos_arr, name):
    nb, rh, cols = gath.shape

    def body(pos_ref, a_ref, b_ref, c_ref, d_ref, o_ref):
        del pos_ref
        o_ref[...] = ((a_ref[...].astype(F32) + b_ref[...].astype(F32)) + c_ref[...].astype(F32)) \
            + d_ref[...].astype(F32)

    tr = rh // 2 if (rh // 2) % 16 == 0 else rh
    specs = [pl.BlockSpec((None, tr, cols), functools.partial(lambda i, pos_ref, j: (j, i, 0), j=j))
             for j in range(nb)]
    return pl.pallas_call(
        body, name=name,
        grid_spec=pltpu.PrefetchScalarGridSpec(
            num_scalar_prefetch=1, grid=(rh // tr,), in_specs=specs,
            out_specs=pl.BlockSpec((None, tr, cols), lambda i, pos_ref: (pos_ref[0], i, 0))),
        out_shape=jax.ShapeDtypeStruct((2, rh, cols), F32), compiler_params=_params(),
    )(pos_arr, gath, gath, gath, gath)


def _small_allreduce(v, name, rider=None, after=()):
    n = v.shape[0]
    n_dev = 8

    def body(v_ref, o_ref, buf, send_sems, recv_sems):
        x, y, c = _mesh_pos()
        me = 4 * x + 2 * y + c
        buf[me] = v_ref[...]
        sends = []
        peers = []
        for r in range(1, n_dev):
            px = 1 - x if r & 4 else x
            py = 1 - y if r & 2 else y
            pc = 1 - c if r & 1 else c
            peers.append((px, py, pc))
            cp = pltpu.make_async_remote_copy(
                src_ref=v_ref, dst_ref=buf.at[me], send_sem=send_sems.at[r - 1],
                recv_sem=recv_sems.at[r - 1], device_id=(px, py, pc), device_id_type=MESH)
            cp.start()
            sends.append(cp)
        for r, (px, py, pc) in enumerate(peers):
            pltpu.make_async_remote_copy(
                src_ref=v_ref, dst_ref=buf.at[4 * px + 2 * py + pc], send_sem=send_sems.at[r],
                recv_sem=recv_sems.at[r], device_id=(px, py, pc), device_id_type=MESH).wait_recv()
        for cp in sends:
            cp.wait_send()
        acc = buf[0]
        for i in range(1, n_dev):
            acc = acc + buf[i]
        o_ref[...] = acc

    whole = pl.BlockSpec(v.shape, lambda i: (0, 0))
    return _pallas(
        body, name=name, grid=(1,), in_specs=[whole], out_specs=whole,
        out_shape=jax.ShapeDtypeStruct(v.shape, v.dtype), operands=[v],
        scratch_shapes=[pltpu.VMEM((n_dev, n, LANES), F32), pltpu.SemaphoreType.DMA((n_dev - 1,)),
                        pltpu.SemaphoreType.DMA((n_dev - 1,))],
        rider=rider, after=after)


def _adamw_math(w, g, m, v):
    m = ADAM_B1 * m + (1.0 - ADAM_B1) * g
    v = ADAM_B2 * v + (1.0 - ADAM_B2) * (g * g)
    m_hat = m / (1.0 - ADAM_B1 ** ADAM_STEP)
    v_hat = v / (1.0 - ADAM_B2 ** ADAM_STEP)
    delta = -ADAM_LR * (m_hat / (jnp.sqrt(v_hat) + ADAM_EPS) + ADAM_WD * w)
    return delta, m, v


def _adamw(w, g, m, v, name, after=()):
    r, c = w.shape
    tr = 128 if r % 128 == 0 else 64
    assert r % tr == 0

    def body(w_ref, g_ref, m_ref, v_ref, go_ref, d_ref, mo_ref, vo_ref):
        gv = g_ref[...]
        d, mn, vn = _adamw_math(w_ref[...], gv, m_ref[...], v_ref[...])
        go_ref[...] = gv
        d_ref[...] = d
        mo_ref[...] = mn
        vo_ref[...] = vn

    blk = pl.BlockSpec((tr, c), lambda i: (i, 0))
    return _pallas(body, name=name, grid=(r // tr,), in_specs=[blk] * 4, out_specs=[blk] * 4,
                   out_shape=[jax.ShapeDtypeStruct((r, c), F32)] * 4, operands=[w, g, m, v], after=after)


def _adamw_small(ws, gs, ms, vs, name):
    n = len(ws)

    def body(*refs):
        w_r, g_r, m_r, v_r = refs[:n], refs[n:2 * n], refs[2 * n:3 * n], refs[3 * n:4 * n]
        d_o, m_o, v_o = refs[4 * n:5 * n], refs[5 * n:6 * n], refs[6 * n:7 * n]
        for i in range(n):
            d, mn, vn = _adamw_math(w_r[i][...], g_r[i][...], m_r[i][...], v_r[i][...])
            d_o[i][...] = d
            m_o[i][...] = mn
            v_o[i][...] = vn

    specs = [pl.BlockSpec(w.shape, lambda i: (0, 0)) for w in ws]
    shapes = [jax.ShapeDtypeStruct(w.shape, F32) for w in ws]
    outs = pl.pallas_call(
        body, name=name, grid=(1,), in_specs=specs * 4, out_specs=specs * 3, out_shape=shapes * 3,
        compiler_params=_params(),
    )(*ws, *gs, *ms, *vs)
    return outs[:n], outs[n:2 * n], outs[2 * n:]


BIG = ("w_in", "w_o_attn", "w_pw_conv", "w_out", "w_ffn_in", "w_ffn_out")
ROW_SHARDED = ("w_out", "w_ffn_out")
SMALL = ("norm1_w", "b_gate", "q_norm_w", "k_norm_w", "conv_w", "conv_b", "conv_ln_w", "conv_ln_b", "norm2_w")
ORDER = ("norm1_w", "w_in", "b_gate", "q_norm_w", "k_norm_w", "w_o_attn", "conv_w", "conv_b", "conv_ln_w",
         "conv_ln_b", "w_pw_conv", "w_out", "norm2_w", "w_ffn_in", "w_ffn_out")
PACK_TILE = 8 * LANES


def _pack_small(parts):
    rows = []
    for p in parts:
        flat = p.reshape(-1)
        pad = (-flat.shape[0]) % PACK_TILE
        rows.append(jnp.pad(flat, (0, pad)).reshape(-1, LANES))
    return jnp.concatenate(rows, axis=0)


def _unpack_small(packed, shapes):
    out, row = [], 0
    for shp in shapes:
        size = int(np.prod(shp))
        nrow = -(-size // PACK_TILE) * (PACK_TILE // LANES)
        out.append(packed[row:row + nrow].reshape(-1)[:size].reshape(shp))
        row += nrow
    return out


def kernel(x, positions, norm1_w, w_in, b_gate, q_norm_w, k_norm_w, w_o_attn, conv_w, conv_b, conv_ln_w, conv_ln_b, w_pw_conv, w_out, norm2_w, w_ffn_in, w_ffn_out, loss_target, m_norm1_w, m_w_in, m_b_gate, m_q_norm_w, m_k_norm_w, m_w_o_attn, m_conv_w, m_conv_b, m_conv_ln_w, m_conv_ln_b, m_w_pw_conv, m_w_out, m_norm2_w, m_w_ffn_in, m_w_ffn_out, v_norm1_w, v_w_in, v_b_gate, v_q_norm_w, v_k_norm_w, v_w_o_attn, v_conv_w, v_conv_b, v_conv_ln_w, v_conv_ln_b, v_w_pw_conv, v_w_out, v_norm2_w, v_w_ffn_in, v_w_ffn_out):
    w = dict(norm1_w=norm1_w, w_in=w_in, b_gate=b_gate, q_norm_w=q_norm_w, k_norm_w=k_norm_w, w_o_attn=w_o_attn,
             conv_w=conv_w, conv_b=conv_b, conv_ln_w=conv_ln_w, conv_ln_b=conv_ln_b, w_pw_conv=w_pw_conv,
             w_out=w_out, norm2_w=norm2_w, w_ffn_in=w_ffn_in, w_ffn_out=w_ffn_out)
    m = dict(norm1_w=m_norm1_w, w_in=m_w_in, b_gate=m_b_gate, q_norm_w=m_q_norm_w, k_norm_w=m_k_norm_w,
             w_o_attn=m_w_o_attn, conv_w=m_conv_w, conv_b=m_conv_b, conv_ln_w=m_conv_ln_w,
             conv_ln_b=m_conv_ln_b, w_pw_conv=m_w_pw_conv, w_out=m_w_out, norm2_w=m_norm2_w,
             w_ffn_in=m_w_ffn_in, w_ffn_out=m_w_ffn_out)
    v = dict(norm1_w=v_norm1_w, w_in=v_w_in, b_gate=v_b_gate, q_norm_w=v_q_norm_w, k_norm_w=v_k_norm_w,
             w_o_attn=v_w_o_attn, conv_w=v_conv_w, conv_b=v_conv_b, conv_ln_w=v_conv_ln_w,
             conv_ln_b=v_conv_ln_b, w_pw_conv=v_w_pw_conv, w_out=v_w_out, norm2_w=v_norm2_w,
             w_ffn_in=v_w_ffn_in, w_ffn_out=v_w_ffn_out)
    cx, cy, cc = _mesh_pos()
    chip = 2 * cx + cy

    chip_arr = chip.reshape(1).astype(jnp.int32)
    pos_arr = jnp.stack([cc, chip]).astype(jnp.int32)
    bufs = {}
    for n in BIG:
        buf = _cast_into_slot(w[n][0], chip_arr, BF16, f"cast_{n}")
        bufs[n] = buf.reshape(N_CHIPS, 2, buf.shape[1] // 2, buf.shape[2])
    small_bufs = [_cast_into_slot(w[n][0], chip_arr, F32, f"slot_{n}") for n in ("conv_w", "b_gate")]
    w_in_buf, conv_w_buf, b_gate_buf = _allgather_inplace([bufs["w_in"]], small_bufs, "allgather_w_in")
    wts = dict(w_in=w_in_buf.reshape(N_CHIPS, -1, w_in_buf.shape[3]),
               conv_w=conv_w_buf.transpose(1, 0, 2).reshape(CONV_WIDTH, -1),
               b_gate=b_gate_buf.transpose(1, 0, 2).reshape(2, 1, -1),
               norm1_w=norm1_w, q_norm_w=q_norm_w, k_norm_w=k_norm_w, conv_b=conv_b, conv_ln_w=conv_ln_w,
               conv_ln_b=conv_ln_b, norm2_w=norm2_w)

    loss, grad_x, g, reduced, w_in_in_flight = _forward_backward(
        x[0], positions.reshape(-1, 1), loss_target[0], wts, [bufs[n] for n in LATE_GATHER], pos_arr)
    grads = {n: b.reshape(-1, b.shape[2]) for n, b in reduced.items()}

    *w_in_in_flight, started = w_in_in_flight
    delta, new_m, new_v = {}, {}, {}
    for n in EARLY_REDUCE:
        grads[n], delta[n], new_m[n], new_v[n] = _adamw(w[n][0], grads[n], m[n][0], v[n][0], f"adamw_{n}",
                                                        after=[started])
    small_parts = [loss] + [g[n] for n in SMALL]
    small_shapes = [p.shape for p in small_parts]
    summed = _small_allreduce(_pack_small(small_parts), "small_allreduce",
                              after=[delta[n] for n in EARLY_REDUCE])
    reduced = _unpack_small(summed, small_shapes)
    loss_total = reduced[0].reshape(())
    for n, r in zip(SMALL, reduced[1:]):
        grads[n] = r
    ch_shard = conv_w.shape[2]
    grads["conv_w"] = lax.dynamic_slice_in_dim(grads["conv_w"], chip * ch_shard, ch_shard, axis=1)
    d_shard = b_gate.shape[2]
    grads["b_gate"] = lax.dynamic_slice_in_dim(grads["b_gate"], chip * d_shard, d_shard, axis=1)

    by_chip_w_in = _chip_exchange_wait(*w_in_in_flight, after=[delta[n] for n in EARLY_REDUCE] + [summed],
                                       name="grads_w_in_exchange_wait")
    half_w_in = _sum_chips(by_chip_w_in, pos_arr, "grads_chip_sum_w_in")
    (shard_w_in,) = _comm_call(_pair_gather_rider([half_w_in]), "grads_pair_gather_w_in")
    grads["w_in"], delta["w_in"], new_m["w_in"], new_v["w_in"] = _adamw(
        w["w_in"][0], shard_w_in.reshape(-1, shard_w_in.shape[2]), m["w_in"][0], v["w_in"][0], "adamw_w_in")
    flat2 = lambda a: a.reshape(-1, a.shape[-1])
    d_s, m_s, v_s = _adamw_small([flat2(w[n]) for n in SMALL], [flat2(grads[n]) for n in SMALL],
                                 [flat2(m[n]) for n in SMALL], [flat2(v[n]) for n in SMALL], "adamw_small")
    for i, n in enumerate(SMALL):
        delta[n], new_m[n], new_v[n] = d_s[i], m_s[i], v_s[i]

    shaped = lambda d, n: d[n].reshape(w[n].shape)
    return (loss_total, grad_x[None], *[shaped(grads, n) for n in ORDER], *[shaped(delta, n) for n in ORDER],
            *[shaped(new_m, n) for n in ORDER], *[shaped(new_v, n) for n in ORDER])
```

```python
import functools

import numpy as np
import jax
import jax.numpy as jnp
from jax import lax
from jax.experimental import pallas as pl
from jax.experimental.pallas import tpu as pltpu

F32 = jnp.float32
BF16 = jnp.bfloat16
MESH = pl.DeviceIdType.MESH
ANY = pl.BlockSpec(memory_space=pl.ANY)

HEAD_DIM = 64
N_SLOT_HEADS = 8
DILATIONS = (1, 4, 16)
HALF_SPAN = 64
ROPE_THETA = 500000.0
ROT_DIM = 16
CONV_WIDTH = 31
EPS = 1e-6
NEG_INF = -1e30
ADAM_LR, ADAM_B1, ADAM_B2, ADAM_EPS, ADAM_WD, ADAM_STEP = 0.001, 0.9, 0.999, 1e-08, 0.01, 10

LANES = 128
QBLK = 128
KWIN = QBLK + 2 * HALF_SPAN
VMEM_LIMIT = 48 * 1024 * 1024
N_CHIPS = 4


def _params(**kw):
    return pltpu.CompilerParams(vmem_limit_bytes=VMEM_LIMIT, **kw)


class _Rider:
    def __init__(self, operands, out_shapes, aliases, scratch, start, wait):
        self.operands, self.out_shapes, self.aliases = list(operands), list(out_shapes), dict(aliases)
        self.scratch, self.start, self.wait = list(scratch), start, wait


def _pallas(body, *, name, grid, in_specs, out_specs, out_shape, operands, scratch_shapes=(), aliases=None,
            rider=None, after=()):
    single = not isinstance(out_specs, (list, tuple))
    out_specs_l = [out_specs] if single else list(out_specs)
    out_shape_l = [out_shape] if single else list(out_shape)
    aliases = dict(aliases or {})
    if rider is None:
        n_main = len(in_specs)

        def ordered(*refs):
            body(*refs[:n_main], *refs[n_main + len(after):])

        res = pl.pallas_call(
            ordered if after else body, name=name, grid=grid, in_specs=list(in_specs) + [ANY] * len(after),
            out_specs=out_specs_l, out_shape=out_shape_l, scratch_shapes=list(scratch_shapes),
            input_output_aliases=aliases, compiler_params=_params(),
        )(*operands, *after)
        return res[0] if single else res
    assert not after
    n_in, n_rin = len(in_specs), len(rider.operands)
    n_out, n_rout = len(out_specs_l), len(rider.out_shapes)
    n_sc = len(scratch_shapes)

    def wrapped(*refs):
        main_in, r_in = refs[:n_in], refs[n_in:n_in + n_rin]
        o0 = n_in + n_rin
        main_out, r_out = refs[o0:o0 + n_out], refs[o0 + n_out:o0 + n_out + n_rout]
        s0 = o0 + n_out + n_rout
        main_sc, r_sc = refs[s0:s0 + n_sc], refs[s0 + n_sc:]
        ids = [pl.program_id(d) for d in range(len(grid))]
        first = functools.reduce(jnp.logical_and, [i == 0 for i in ids])
        last = functools.reduce(jnp.logical_and, [i == n - 1 for i, n in zip(ids, grid)])

        @pl.when(first)
        def _():
            rider.start(r_in, r_out, r_sc)

        body(*main_in, *main_out, *main_sc)

        @pl.when(last)
        def _():
            rider.wait(r_in, r_out, r_sc)

    for src, dst in rider.aliases.items():
        aliases[n_in + src] = n_out + dst
    res = pl.pallas_call(
        wrapped, name=name, grid=grid, in_specs=list(in_specs) + [ANY] * n_rin,
        out_specs=out_specs_l + [ANY] * n_rout, out_shape=out_shape_l + rider.out_shapes,
        scratch_shapes=list(scratch_shapes) + rider.scratch, input_output_aliases=aliases,
        compiler_params=_params(),
    )(*operands, *rider.operands)
    main = res[:n_out]
    return (main[0] if single else main), res[n_out:]


def _matmul(a, b, *, mode, tm, tn, tk, out_dtype, name, b_blocked=False,
            out_blocked=None, rider=None, after=()):
    a_shape = a.shape
    if mode == "nn":
        m_dim, k_dim = a_shape
        n_dim = b.shape[0] * b.shape[2] if b_blocked else b.shape[1]
        rows, cols, red = m_dim, n_dim, k_dim
    elif mode == "nt":
        m_dim, n_dim = a_shape
        k_dim = b.shape[1] if b_blocked else b.shape[0]
        rows, cols, red = m_dim, k_dim, n_dim
    else:
        m_dim, k_dim = a_shape
        n_dim = b.shape[1]
        rows, cols, red = k_dim, n_dim, m_dim
    assert rows % tm == 0 and cols % tn == 0 and red % tk == 0, (name, rows, cols, red)
    ni, nj, nk = rows // tm, cols // tn, red // tk

    if mode == "nn":
        a_spec = pl.BlockSpec((tm, tk), lambda i, j, k: (i, k))
        if b_blocked:
            per = b.shape[2] // tn
            b_spec = pl.BlockSpec((None, tk, tn), lambda i, j, k: (j // per, k, j % per))
        else:
            b_spec = pl.BlockSpec((tk, tn), lambda i, j, k: (k, j))
        dims = (((1,), (0,)), ((), ()))
    elif mode == "nt":
        a_spec = pl.BlockSpec((tm, tk), lambda i, j, k: (i, k))
        if b_blocked:
            per = b.shape[2] // tk
            b_spec = pl.BlockSpec((None, tn, tk), lambda i, j, k: (k // per, j, k % per))
        else:
            b_spec = pl.BlockSpec((tn, tk), lambda i, j, k: (j, k))
        dims = (((1,), (1,)), ((), ()))
    else:
        a_spec = pl.BlockSpec((tk, tm), lambda i, j, k: (k, i))
        b_spec = pl.BlockSpec((tk, tn), lambda i, j, k: (k, j))
        dims = (((0,), (0,)), ((), ()))

    if out_blocked:
        per_o = (cols // out_blocked) // tn
        out_spec = pl.BlockSpec((None, tm, tn), lambda i, j, k: (j // per_o, i, j % per_o))
        out_shape = jax.ShapeDtypeStruct((out_blocked, rows, cols // out_blocked), out_dtype)
    else:
        out_spec = pl.BlockSpec((tm, tn), lambda i, j, k: (i, j))
        out_shape = jax.ShapeDtypeStruct((rows, cols), out_dtype)

    def body(a_ref, b_ref, o_ref, *acc):
        prod = lax.dot_general(a_ref[...], b_ref[...], dims, preferred_element_type=F32)
        if nk == 1:
            o_ref[...] = prod.astype(out_dtype)
        else:
            acc_ref, = acc
            k = pl.program_id(2)

            @pl.when(k == 0)
            def _():
                acc_ref[...] = prod

            @pl.when(k > 0)
            def _():
                acc_ref[...] += prod

            @pl.when(k == nk - 1)
            def _():
                o_ref[...] = acc_ref[...].astype(out_dtype)

    scratch = [pltpu.VMEM((tm, tn), F32)] if nk > 1 else []
    return _pallas(body, name=name, grid=(ni, nj, nk), in_specs=[a_spec, b_spec], out_specs=out_spec,
                   out_shape=out_shape, operands=[a, b], scratch_shapes=scratch, rider=rider, after=after)


def _rmsnorm_fwd(x, w, name):
    s, d = x.shape
    tm = 256

    def body(x_ref, w_ref, o_ref):
        xv = x_ref[...]
        rstd = lax.rsqrt(jnp.mean(xv * xv, axis=-1, keepdims=True) + EPS)
        o_ref[...] = (xv * rstd * w_ref[...]).astype(BF16)

    return pl.pallas_call(
        body, name=name, grid=(s // tm,),
        in_specs=[pl.BlockSpec((tm, d), lambda i: (i, 0)), pl.BlockSpec((1, d), lambda i: (0, 0))],
        out_specs=pl.BlockSpec((tm, d), lambda i: (i, 0)),
        out_shape=jax.ShapeDtypeStruct((s, d), BF16), compiler_params=_params(),
    )(x, w)


def _rmsnorm_bwd(dh, x, w, dres, name, rider=None):
    s, d = x.shape
    tm = 256

    def body(dh_ref, x_ref, w_ref, dres_ref, dx_ref, dxb_ref, dw_ref):
        xv = x_ref[...]
        rstd = lax.rsqrt(jnp.mean(xv * xv, axis=-1, keepdims=True) + EPS)
        xhat = xv * rstd
        dhv = dh_ref[...]
        g = dhv * w_ref[...]
        dx = rstd * (g - xhat * jnp.mean(g * xhat, axis=-1, keepdims=True)) + dres_ref[...]
        dx_ref[...] = dx
        dxb_ref[...] = dx.astype(BF16)
        part = jnp.sum(dhv * xhat, axis=0, keepdims=True)

        @pl.when(pl.program_id(0) == 0)
        def _():
            dw_ref[...] = part

        @pl.when(pl.program_id(0) > 0)
        def _():
            dw_ref[...] += part

    row = pl.BlockSpec((tm, d), lambda i: (i, 0))
    vec = pl.BlockSpec((1, d), lambda i: (0, 0))
    return _pallas(
        body, name=name, grid=(s // tm,), in_specs=[row, row, vec, row], out_specs=[row, row, vec],
        out_shape=[jax.ShapeDtypeStruct((s, d), F32), jax.ShapeDtypeStruct((s, d), BF16),
                   jax.ShapeDtypeStruct((1, d), F32)],
        operands=[dh, x, w, dres], rider=rider)


def _rope_consts():
    lane = np.arange(LANES)
    in_head = lane % HEAD_DIM
    inv_freq = ROPE_THETA ** (-jnp.arange(0, ROT_DIM, 2, dtype=F32) / ROT_DIM)
    invf = jnp.where(jnp.asarray(in_head < ROT_DIM), jnp.tile(inv_freq, LANES // (ROT_DIM // 2)), 0.0)
    m_a = np.where(in_head < ROT_DIM // 2, -1.0, 0.0).astype(np.float32)
    m_b = np.where((in_head >= ROT_DIM // 2) & (in_head < ROT_DIM), 1.0, 0.0).astype(np.float32)
    block_diag = (lane[:, None] // HEAD_DIM == lane[None, :] // HEAD_DIM).astype(np.float32)
    return (invf.reshape(1, LANES).astype(F32), jnp.asarray(m_a).reshape(1, LANES),
            jnp.asarray(m_b).reshape(1, LANES), jnp.asarray(block_diag, dtype=BF16))


def _head_sums(v, bd):
    hi = v.astype(BF16)
    lo = (v - hi.astype(F32)).astype(BF16)
    return jnp.dot(hi, bd, preferred_element_type=F32) + jnp.dot(lo, bd, preferred_element_type=F32)


def _qk_fwd(proj, pos_col, qw2, kw2, consts, name, rider=None):
    s = proj.shape[0]
    width = 3 * N_SLOT_HEADS * HEAD_DIM
    tm = 128
    invf, m_a, m_b, bd = consts
    scale = HEAD_DIM ** -0.5

    def body(q_ref, k_ref, pos_ref, qw_ref, kw_ref, invf_ref, ma_ref, mb_ref, bd_ref, qo_ref, ko_ref):
        ang = pos_ref[...].astype(F32) * invf_ref[...]
        cos = jnp.cos(ang)
        sin = jnp.sin(ang)
        s_a = sin * ma_ref[...]
        s_b = sin * mb_ref[...]
        bdv = bd_ref[...]
        for src, w_ref, dst, sc in ((q_ref, qw_ref, qo_ref, scale), (k_ref, kw_ref, ko_ref, 1.0)):
            for cb in range(width // LANES):
                cols = slice(cb * LANES, (cb + 1) * LANES)
                t = src[:, cols]
                rstd = lax.rsqrt(_head_sums(t * t, bdv) * (1.0 / HEAD_DIM) + EPS)
                y = t * rstd * w_ref[...]
                r = y * cos + pltpu.roll(y, LANES - 8, axis=1) * s_a + pltpu.roll(y, 8, axis=1) * s_b
                dst[:, cols] = r * sc if sc != 1.0 else r

    vec = pl.BlockSpec((1, LANES), lambda i: (0, 0))
    return _pallas(
        body, name=name, grid=(s // tm,),
        in_specs=[pl.BlockSpec((tm, width), lambda i: (i, 0)), pl.BlockSpec((tm, width), lambda i: (i, 1)),
                  pl.BlockSpec((tm, 1), lambda i: (i, 0)), vec, vec, vec, vec, vec,
                  pl.BlockSpec((LANES, LANES), lambda i: (0, 0))],
        out_specs=[pl.BlockSpec((tm, width), lambda i: (i, 0))] * 2,
        out_shape=[jax.ShapeDtypeStruct((s, width), F32)] * 2,
        operands=[proj, proj, pos_col, qw2, kw2, invf, m_a, m_b, bd], rider=rider)


def _qk_bwd(dqn, dkn, dv, da, db, dgl, proj, pos_col, qw2, kw2, consts, name, rider=None):
    s = proj.shape[0]
    width = 3 * N_SLOT_HEADS * HEAD_DIM
    ch = da.shape[1]
    gate_w = dgl.shape[1]
    out_w = 3 * width + 2 * ch + gate_w
    assert out_w == proj.shape[1]
    tm = 128
    invf, m_a, m_b, bd = consts
    scale = HEAD_DIM ** -0.5

    def body(dq_ref, dk_ref, dv_ref, da_ref, db_ref, dgl_ref, q_ref, k_ref, pos_ref, qw_ref, kw_ref,
             invf_ref, ma_ref, mb_ref, bd_ref, out_ref, dqw_ref, dkw_ref):
        ang = pos_ref[...].astype(F32) * invf_ref[...]
        cos = jnp.cos(ang)
        sin = jnp.sin(ang)
        s_a = sin * ma_ref[...]
        s_b = sin * mb_ref[...]
        bdv = bd_ref[...]
        first = pl.program_id(0) == 0
        for src, dsrc, w_ref, col0, dw_ref, sc in ((q_ref, dq_ref, qw_ref, 0, dqw_ref, scale),
                                                   (k_ref, dk_ref, kw_ref, width, dkw_ref, 1.0)):
            dw_acc = jnp.zeros((1, LANES), F32)
            for cb in range(width // LANES):
                cols = slice(cb * LANES, (cb + 1) * LANES)
                t = src[:, cols]
                dr = dsrc[:, cols]
                if sc != 1.0:
                    dr = dr * sc
                dy = dr * cos + pltpu.roll(dr * s_a, 8, axis=1) + pltpu.roll(dr * s_b, LANES - 8, axis=1)
                rstd = lax.rsqrt(_head_sums(t * t, bdv) * (1.0 / HEAD_DIM) + EPS)
                xhat = t * rstd
                g = dy * w_ref[...]
                dt = rstd * (g - xhat * (_head_sums(g * xhat, bdv) * (1.0 / HEAD_DIM)))
                out_ref[:, col0 + cb * LANES: col0 + (cb + 1) * LANES] = dt.astype(BF16)
                dw_acc = dw_acc + jnp.sum(dy * xhat, axis=0, keepdims=True)
            dw_acc = dw_acc + pltpu.roll(dw_acc, HEAD_DIM, axis=1)

            @pl.when(first)
            def _(dw_ref=dw_ref, dw_acc=dw_acc):
                dw_ref[...] = dw_acc

            @pl.when(jnp.logical_not(first))
            def _(dw_ref=dw_ref, dw_acc=dw_acc):
                dw_ref[...] += dw_acc
        out_ref[:, 2 * width: 3 * width] = dv_ref[...].astype(BF16)
        out_ref[:, 3 * width: 3 * width + ch] = da_ref[...]
        out_ref[:, 3 * width + ch: 3 * width + 2 * ch] = db_ref[...]
        out_ref[:, 3 * width + 2 * ch: out_w] = dgl_ref[...]

    vec = pl.BlockSpec((1, LANES), lambda i: (0, 0))
    blk = lambda c: pl.BlockSpec((tm, width), lambda i: (i, c))
    cblk = pl.BlockSpec((tm, ch), lambda i: (i, 0))
    return _pallas(
        body, name=name, grid=(s // tm,),
        in_specs=[blk(0), blk(0), blk(0), cblk, cblk, pl.BlockSpec((tm, gate_w), lambda i: (i, 0)),
                  blk(0), blk(1), pl.BlockSpec((tm, 1), lambda i: (i, 0)), vec, vec, vec, vec, vec,
                  pl.BlockSpec((LANES, LANES), lambda i: (0, 0))],
        out_specs=[pl.BlockSpec((tm, out_w), lambda i: (i, 0)), vec, vec],
        out_shape=[jax.ShapeDtypeStruct((s, out_w), BF16)] + [jax.ShapeDtypeStruct((1, LANES), F32)] * 2,
        operands=[dqn, dkn, dv, da, db, dgl, proj, proj, pos_col, qw2, kw2, invf, m_a, m_b, bd],
        rider=rider)


def _row_chunks(n_rows, fn, chunk=256):
    def step(i, c):
        fn(pl.ds(pl.multiple_of(i * chunk, chunk), chunk))
        return c
    lax.fori_loop(0, n_rows // chunk, step, 0)


def _to_residue_major(dst, src, s, d, dst_off=0, cast=None):
    seq = s // d
    for r in range(d):
        v = src[...] if d == 1 else src[pl.ds(r, seq, stride=d), :]
        dst[dst_off + r * seq: dst_off + (r + 1) * seq, :] = v if cast is None else v.astype(cast)


def _from_residue_major(dst, src, s, d, src_off=0):
    seq = s // d
    for r in range(d):
        v = src[src_off + r * seq: src_off + (r + 1) * seq, :]
        if d == 1:
            dst[...] = v
        else:
            dst[pl.ds(r, seq, stride=d), :] = v


def _band_bias():
    qi = lax.broadcasted_iota(jnp.int32, (QBLK, KWIN), 0)
    kj = lax.broadcasted_iota(jnp.int32, (QBLK, KWIN), 1)
    return jnp.where(jnp.abs(kj - HALF_SPAN - qi) <= HALF_SPAN, 0.0, NEG_INF).astype(F32)


def _range_bias(base, seq):
    kj = lax.broadcasted_iota(jnp.int32, (1, KWIN), 1)
    lo = (base & -seq) - base + HALF_SPAN
    return jnp.where((kj >= lo) & (kj < lo + seq), 0.0, NEG_INF).astype(F32)


ATTN_BLOCKS_PER_TRIP = 4


def _skewed_blocks(n_blk, produce, consume):
    per = ATTN_BLOCKS_PER_TRIP
    produce(0, 0)

    def trip(i, carry):
        for u in range(per):
            consume(per * i + u, u % 2)
            produce(per * i + u + 1, (u + 1) % 2)
        return carry

    n_trips = n_blk // per - 1
    lax.fori_loop(0, n_trips, trip, 0)
    for b in range(per * n_trips, n_blk):
        consume(b, b % 2)
        if b + 1 < n_blk:
            produce(b + 1, (b + 1) % 2)


def _block_base(b):
    return b * QBLK if isinstance(b, int) else pl.multiple_of(b * QBLK, QBLK)


def _attn_fwd(qn, kn, proj, name, rider=None):
    s = qn.shape[0]
    n_pairs = N_SLOT_HEADS * HEAD_DIM // LANES
    v_col0 = 2 * qn.shape[1] // LANES
    nt_dims = (((1,), (1,)), ((), ()))

    def body(q_ref, k_ref, v_ref, attn_ref, lse_ref, attn_b_ref, q_rm, k_rm, v_rm, acc_rm, m_rm, l_rm,
             acc_p, m_p, l_p, m_run, l_run, acc_run, band, s_buf, m_buf):
        g = pl.program_id(1)
        zpad = jnp.zeros((HALF_SPAN, LANES), BF16)
        k_rm[0:HALF_SPAN, :] = zpad
        k_rm[s + HALF_SPAN: s + 2 * HALF_SPAN, :] = zpad
        v_rm[0:HALF_SPAN, 0:LANES] = zpad
        v_rm[s + HALF_SPAN: s + 2 * HALF_SPAN, 0:LANES] = zpad

        def ones_rows(rows):
            v_rm[pl.ds(rows.start, rows.size), LANES:2 * LANES] = jnp.ones((rows.size, LANES), BF16)

        _row_chunks(s + 2 * HALF_SPAN, ones_rows, chunk=2 * HALF_SPAN)
        band[...] = _band_bias()
        lane = lax.broadcasted_iota(jnp.int32, (QBLK, LANES), 1)
        low = lane < HEAD_DIM
        n_blk = s // QBLK

        for gi, d in enumerate(DILATIONS):
            @pl.when(g == gi)
            def _(gi=gi, d=d):
                seq = s // d
                _to_residue_major(q_rm, q_ref, s, d, cast=BF16)
                _to_residue_major(k_rm, k_ref, s, d, dst_off=HALF_SPAN, cast=BF16)
                _to_residue_major(v_rm.at[:, 0:LANES], v_ref, s, d, dst_off=HALF_SPAN, cast=BF16)

                def scores(b, slot):
                    base = _block_base(b)
                    q = q_rm[pl.ds(base, QBLK), :]
                    zero = jnp.zeros_like(q)
                    q2 = jnp.concatenate([jnp.where(low, q, zero), jnp.where(low, zero, q)], axis=0)
                    sc = lax.dot_general(q2, k_rm[pl.ds(base, KWIN), :], nt_dims, preferred_element_type=F32)
                    bias = band[...] + _range_bias(base, seq)
                    for hh in range(2):
                        rows = slice(hh * QBLK, (hh + 1) * QBLK)
                        sh = sc[rows, :] + bias
                        s_buf[slot, rows, :] = sh
                        m_buf[slot, rows, :] = jnp.broadcast_to(jnp.max(sh, axis=-1, keepdims=True), (QBLK, LANES))

                def outputs(b, slot):
                    base = _block_base(b)
                    sv = s_buf[slot]
                    mb = m_buf[slot]
                    p = jnp.exp(jnp.concatenate([sv[:, 0:LANES] - mb, sv[:, LANES:2 * LANES] - mb], axis=1))
                    pv = jnp.dot(p.astype(BF16), v_rm[pl.ds(base, KWIN), :], preferred_element_type=F32)
                    rows = pl.ds(base, QBLK)
                    acc_rm[rows, :] = jnp.where(low, pv[0:QBLK, 0:LANES], pv[QBLK:2 * QBLK, 0:LANES])
                    l_rm[rows, :] = jnp.where(low, pv[0:QBLK, LANES:2 * LANES], pv[QBLK:2 * QBLK, LANES:2 * LANES])
                    m_rm[rows, :] = jnp.where(low, mb[0:QBLK, :], mb[QBLK:2 * QBLK, :])

                _skewed_blocks(n_blk, scores, outputs)
                if d == 1:
                    src = (acc_rm, m_rm, l_rm)
                else:
                    for dst_, src_ in ((acc_p, acc_rm), (m_p, m_rm), (l_p, l_rm)):
                        _from_residue_major(dst_, src_, s, d)
                    src = (acc_p, m_p, l_p)

                def combine(rows):
                    a_g, m_g, l_g = src[0][rows, :], src[1][rows, :], src[2][rows, :]
                    if gi == 0:
                        m_new, l_new, a_new = m_g, l_g, a_g
                    else:
                        m_old = m_run[rows, :]
                        m_new = jnp.maximum(m_old, m_g)
                        w_old = jnp.exp(m_old - m_new)
                        w_g = jnp.exp(m_g - m_new)
                        l_new = l_run[rows, :] * w_old + l_g * w_g
                        a_new = acc_run[rows, :] * w_old + a_g * w_g
                    if gi == len(DILATIONS) - 1:
                        out = a_new / l_new
                        attn_ref[rows, :] = out
                        attn_b_ref[rows, :] = out.astype(BF16)
                        lse_ref[rows, :] = m_new + jnp.log(l_new)
                    else:
                        m_run[rows, :] = m_new
                        l_run[rows, :] = l_new
                        acc_run[rows, :] = a_new

                _row_chunks(s, combine)

    qk_spec = pl.BlockSpec((s, LANES), lambda hp, g: (0, g * n_pairs + hp))
    v_spec = pl.BlockSpec((s, LANES), lambda hp, g: (0, v_col0 + g * n_pairs + hp))
    o_spec = pl.BlockSpec((s, LANES), lambda hp, g: (0, hp))
    f32buf = pltpu.VMEM((s, LANES), F32)
    return _pallas(
        body, name=name, grid=(n_pairs, len(DILATIONS)), in_specs=[qk_spec, qk_spec, v_spec],
        out_specs=[o_spec, o_spec, o_spec],
        out_shape=[jax.ShapeDtypeStruct((s, n_pairs * LANES), F32)] * 2
        + [jax.ShapeDtypeStruct((s, n_pairs * LANES), BF16)],
        operands=[qn, kn, proj],
        scratch_shapes=[pltpu.VMEM((s, LANES), BF16), pltpu.VMEM((s + 2 * HALF_SPAN, LANES), BF16),
                        pltpu.VMEM((s + 2 * HALF_SPAN, 2 * LANES), BF16)] + [f32buf] * 9
        + [pltpu.VMEM((QBLK, KWIN), F32), pltpu.VMEM((2, 2 * QBLK, KWIN), F32),
           pltpu.VMEM((2, 2 * QBLK, LANES), F32)],
        rider=rider)


def _attn_bwd(qn, kn, proj, dattn, attn, lse, bd, name, rider=None):
    s = qn.shape[0]
    n_pairs = N_SLOT_HEADS * HEAD_DIM // LANES
    v_col0 = 2 * qn.shape[1] // LANES
    nt_dims = (((1,), (1,)), ((), ()))
    tn_dims = (((0,), (0,)), ((), ()))
    spad = s + 2 * HALF_SPAN

    def body(q_ref, k_ref, v_ref, do_ref, o_ref, lse_ref, bd_ref, dq_ref, dk_ref, dv_ref,
             q_rm, k_rm, v_rm, do_rm, lse0_rm, lse1_rm, dd0_rm, dd1_rm, dq_rm, dk_rm, dv_rm,
             lse0_p, lse1_p, dd0_p, dd1_p, band, p_buf, ds_buf):
        g = pl.program_id(1)
        zpad = jnp.zeros((HALF_SPAN, LANES), BF16)
        for buf in (k_rm, v_rm):
            buf[0:HALF_SPAN, :] = zpad
            buf[s + HALF_SPAN: spad, :] = zpad
        zf = jnp.zeros((HALF_SPAN, LANES), F32)
        for buf in (dk_rm, dv_rm):
            buf[0:HALF_SPAN, :] = zf
            buf[s + HALF_SPAN: spad, :] = zf
        band[...] = _band_bias()

        def clear(rows):
            z = jnp.zeros((rows.size, LANES), F32)
            dk_rm[pl.ds(rows.start + HALF_SPAN, rows.size), :] = z
            dv_rm[pl.ds(rows.start + HALF_SPAN, rows.size), :] = z

        _row_chunks(s, clear)

        def prepare(rows):
            lo = lax.broadcasted_iota(jnp.int32, (rows.size, LANES), 1) < HEAD_DIM
            dsum = _head_sums(do_ref[rows, :] * o_ref[rows, :], bd_ref[...])
            dswap = pltpu.roll(dsum, HEAD_DIM, axis=1)
            dd0_p[rows, :] = jnp.where(lo, dsum, dswap)
            dd1_p[rows, :] = jnp.where(lo, dswap, dsum)
            lv = lse_ref[rows, :]
            lswap = pltpu.roll(lv, HEAD_DIM, axis=1)
            lse0_p[rows, :] = jnp.where(lo, lv, lswap)
            lse1_p[rows, :] = jnp.where(lo, lswap, lv)

        @pl.when(g == 0)
        def _():
            _row_chunks(s, prepare)
        lane = lax.broadcasted_iota(jnp.int32, (QBLK, LANES), 1)
        low = lane < HEAD_DIM
        n_blk = s // QBLK

        def stacked(ref, rows):
            val = ref[rows, :]
            zero = jnp.zeros_like(val)
            return jnp.concatenate([jnp.where(low, val, zero), jnp.where(low, zero, val)], axis=0)

        for gi, d in enumerate(DILATIONS):
            @pl.when(g == gi)
            def _(d=d):
                seq = s // d
                _to_residue_major(q_rm, q_ref, s, d, cast=BF16)
                _to_residue_major(k_rm, k_ref, s, d, dst_off=HALF_SPAN, cast=BF16)
                _to_residue_major(v_rm, v_ref, s, d, dst_off=HALF_SPAN, cast=BF16)
                _to_residue_major(do_rm, do_ref, s, d, cast=BF16)
                for dst_, src_ in ((lse0_rm, lse0_p), (lse1_rm, lse1_p), (dd0_rm, dd0_p), (dd1_rm, dd1_p)):
                    _to_residue_major(dst_, src_, s, d)

                def scores(b, slot):
                    base = _block_base(b)
                    rows = pl.ds(base, QBLK)
                    win = pl.ds(base, KWIN)
                    sc = lax.dot_general(stacked(q_rm, rows), k_rm[win, :], nt_dims, preferred_element_type=F32)
                    dp = lax.dot_general(stacked(do_rm, rows), v_rm[win, :], nt_dims, preferred_element_type=F32)
                    bias = band[...] + _range_bias(base, seq)
                    for hh, (lse_r, dd_r) in enumerate(((lse0_rm, dd0_rm), (lse1_rm, dd1_rm))):
                        r = slice(hh * QBLK, (hh + 1) * QBLK)
                        lse_h = lse_r[rows, :]
                        dd_h = dd_r[rows, :]
                        sh = sc[r, :] + bias
                        p = jnp.exp(jnp.concatenate([sh[:, 0:LANES] - lse_h, sh[:, LANES:KWIN] - lse_h], axis=1))
                        dph = dp[r, :]
                        ds = p * jnp.concatenate([dph[:, 0:LANES] - dd_h, dph[:, LANES:KWIN] - dd_h], axis=1)
                        p_buf[slot, r, :] = p.astype(BF16)
                        ds_buf[slot, r, :] = ds.astype(BF16)

                def grads(b, slot):
                    base = _block_base(b)
                    rows = pl.ds(base, QBLK)
                    win = pl.ds(base, KWIN)
                    p = p_buf[slot]
                    ds = ds_buf[slot]
                    dq2 = jnp.dot(ds, k_rm[win, :], preferred_element_type=F32)
                    dq_rm[rows, :] = jnp.where(low, dq2[0:QBLK, :], dq2[QBLK:2 * QBLK, :])
                    dk_rm[win, :] += lax.dot_general(ds, stacked(q_rm, rows), tn_dims, preferred_element_type=F32)
                    dv_rm[win, :] += lax.dot_general(p, stacked(do_rm, rows), tn_dims, preferred_element_type=F32)

                _skewed_blocks(n_blk, scores, grads)
                _from_residue_major(dq_ref, dq_rm, s, d)
                _from_residue_major(dk_ref, dk_rm, s, d, src_off=HALF_SPAN)
                _from_residue_major(dv_ref, dv_rm, s, d, src_off=HALF_SPAN)

    qk_spec = pl.BlockSpec((s, LANES), lambda hp, g: (0, g * n_pairs + hp))
    v_spec = pl.BlockSpec((s, LANES), lambda hp, g: (0, v_col0 + g * n_pairs + hp))
    o_spec = pl.BlockSpec((s, LANES), lambda hp, g: (0, hp))
    width = qn.shape[1]
    f32buf = pltpu.VMEM((s, LANES), F32)
    f32pad = pltpu.VMEM((spad, LANES), F32)
    return _pallas(
        body, name=name, grid=(n_pairs, len(DILATIONS)),
        in_specs=[qk_spec, qk_spec, v_spec, o_spec, o_spec, o_spec,
                  pl.BlockSpec((LANES, LANES), lambda hp, g: (0, 0))],
        out_specs=[qk_spec, qk_spec, qk_spec],
        out_shape=[jax.ShapeDtypeStruct((s, width), F32)] * 3,
        operands=[qn, kn, proj, dattn, attn, lse, bd],
        scratch_shapes=[pltpu.VMEM((s, LANES), BF16), pltpu.VMEM((spad, LANES), BF16),
                        pltpu.VMEM((spad, LANES), BF16), pltpu.VMEM((s, LANES), BF16),
                        f32buf, f32buf, f32buf, f32buf, f32buf, f32pad, f32pad,
                        f32buf, f32buf, f32buf, f32buf, pltpu.VMEM((QBLK, KWIN), F32),
                        pltpu.VMEM((2, 2 * QBLK, KWIN), BF16), pltpu.VMEM((2, 2 * QBLK, KWIN), BF16)],
        rider=rider)


CONV_PAD = 16


def _conv_fwd(proj, conv_w, conv_b, col0, name, rider=None):
    s = proj.shape[0]
    ch = conv_w.shape[1]
    nblk = ch // LANES
    a0 = col0 // LANES
    tr = 256
    shift = CONV_PAD - (CONV_WIDTH - 1) // 2

    def body(a_ref, b_ref, w_ref, bias_ref, u0_ref, uc_ref, pad):
        z = jnp.zeros((CONV_PAD, LANES), F32)
        pad[0:CONV_PAD, :] = z
        pad[s + CONV_PAD: s + 2 * CONV_PAD, :] = z

        def glu(rows):
            u0 = a_ref[rows, :] * jax.nn.sigmoid(b_ref[rows, :])
            u0_ref[rows, :] = u0
            pad[pl.ds(rows.start + CONV_PAD, rows.size), :] = u0

        _row_chunks(s, glu)
        for t in range(0, s, tr):
            acc = jnp.broadcast_to(bias_ref[...], (tr, LANES))
            for k in range(CONV_WIDTH):
                acc = acc + w_ref[k:k + 1, :] * pad[t + k + shift: t + k + shift + tr, :]
            uc_ref[t:t + tr, :] = acc

    return _pallas(
        body, name=name, grid=(nblk,),
        in_specs=[pl.BlockSpec((s, LANES), lambda c: (0, a0 + c)),
                  pl.BlockSpec((s, LANES), lambda c: (0, a0 + nblk + c)),
                  pl.BlockSpec((CONV_WIDTH, LANES), lambda c: (0, c)),
                  pl.BlockSpec((1, LANES), lambda c: (0, c))],
        out_specs=[pl.BlockSpec((s, LANES), lambda c: (0, c))] * 2,
        out_shape=[jax.ShapeDtypeStruct((s, ch), F32)] * 2, operands=[proj, proj, conv_w, conv_b],
        scratch_shapes=[pltpu.VMEM((s + 2 * CONV_PAD, LANES), F32)], rider=rider)


def _ln_silu_fwd(uc, ln_w, ln_b, name):
    s, ch = uc.shape
    tm = 256

    def body(u_ref, w_ref, b_ref, o_ref):
        u = u_ref[...]
        mu = jnp.mean(u, axis=-1, keepdims=True)
        xc = u - mu
        rstd = lax.rsqrt(jnp.mean(xc * xc, axis=-1, keepdims=True) + EPS)
        z = xc * rstd * w_ref[...] + b_ref[...]
        o_ref[...] = (z * jax.nn.sigmoid(z)).astype(BF16)

    row = pl.BlockSpec((tm, ch), lambda i: (i, 0))
    vec = pl.BlockSpec((1, ch), lambda i: (0, 0))
    return pl.pallas_call(
        body, name=name, grid=(s // tm,), in_specs=[row, vec, vec], out_specs=row,
        out_shape=jax.ShapeDtypeStruct((s, ch), BF16), compiler_params=_params(),
    )(uc, ln_w, ln_b)


def _ln_silu_bwd(du3, uc, ln_w, ln_b, name):
    s, ch = uc.shape
    tm = 256

    def body(d_ref, u_ref, w_ref, b_ref, du_ref, dw_ref, db_ref):
        u = u_ref[...]
        mu = jnp.mean(u, axis=-1, keepdims=True)
        xc = u - mu
        rstd = lax.rsqrt(jnp.mean(xc * xc, axis=-1, keepdims=True) + EPS)
        xhat = xc * rstd
        z = xhat * w_ref[...] + b_ref[...]
        sg = jax.nn.sigmoid(z)
        dz = d_ref[...] * (sg * (1.0 + z * (1.0 - sg)))
        dxh = dz * w_ref[...]
        du_ref[...] = rstd * (dxh - jnp.mean(dxh, axis=-1, keepdims=True)
                              - xhat * jnp.mean(dxh * xhat, axis=-1, keepdims=True))
        pw = jnp.sum(dz * xhat, axis=0, keepdims=True)
        pb = jnp.sum(dz, axis=0, keepdims=True)
        first = pl.program_id(0) == 0

        @pl.when(first)
        def _():
            dw_ref[...] = pw
            db_ref[...] = pb

        @pl.when(jnp.logical_not(first))
        def _():
            dw_ref[...] += pw
            db_ref[...] += pb

    row = pl.BlockSpec((tm, ch), lambda i: (i, 0))
    vec = pl.BlockSpec((1, ch), lambda i: (0, 0))
    return pl.pallas_call(
        body, name=name, grid=(s // tm,), in_specs=[row, row, vec, vec], out_specs=[row, vec, vec],
        out_shape=[jax.ShapeDtypeStruct((s, ch), F32), jax.ShapeDtypeStruct((1, ch), F32),
                   jax.ShapeDtypeStruct((1, ch), F32)],
        compiler_params=_params(),
    )(du3, uc, ln_w, ln_b)


def _conv_bwd(duc, u0, proj, conv_w, col0, name, rider=None):
    s = proj.shape[0]
    ch = conv_w.shape[1]
    nblk = ch // LANES
    a0 = col0 // LANES
    tr = 256
    half = (CONV_WIDTH - 1) // 2
    shift = CONV_PAD - half

    def body(duc_ref, u0_ref, a_ref, b_ref, w_ref, da_ref, db_ref, dw_ref, dbias_ref, pad_d, pad_u):
        z = jnp.zeros((CONV_PAD, LANES), F32)
        for buf in (pad_d, pad_u):
            buf[0:CONV_PAD, :] = z
            buf[s + CONV_PAD: s + 2 * CONV_PAD, :] = z

        def fill(rows):
            dst = pl.ds(rows.start + CONV_PAD, rows.size)
            pad_d[dst, :] = duc_ref[rows, :]
            pad_u[dst, :] = u0_ref[rows, :]

        _row_chunks(s, fill)
        dw_acc = [jnp.zeros((8, LANES), F32) for _ in range(CONV_WIDTH)]
        dbias_acc = jnp.zeros((8, LANES), F32)
        for t in range(0, s, tr):
            d_t = duc_ref[t:t + tr, :]
            dbias_acc = dbias_acc + jnp.sum(d_t.reshape(tr // 8, 8, LANES), axis=0)
            du0 = jnp.zeros((tr, LANES), F32)
            for k in range(CONV_WIDTH):
                du0 = du0 + w_ref[k:k + 1, :] * pad_d[t - k + half + CONV_PAD: t - k + half + CONV_PAD + tr, :]
                prod = d_t * pad_u[t + k + shift: t + k + shift + tr, :]
                dw_acc[k] = dw_acc[k] + jnp.sum(prod.reshape(tr // 8, 8, LANES), axis=0)
            av = a_ref[t:t + tr, :]
            sg = jax.nn.sigmoid(b_ref[t:t + tr, :])
            da_ref[t:t + tr, :] = (du0 * sg).astype(BF16)
            db_ref[t:t + tr, :] = (du0 * av * sg * (1.0 - sg)).astype(BF16)
        for k in range(CONV_WIDTH):
            dw_ref[k:k + 1, :] = jnp.sum(dw_acc[k], axis=0, keepdims=True)
        dbias_ref[...] = jnp.sum(dbias_acc, axis=0, keepdims=True)

    col = lambda off: pl.BlockSpec((s, LANES), lambda c: (0, off + c))
    return _pallas(
        body, name=name, grid=(nblk,),
        in_specs=[col(0), col(0), col(a0), col(a0 + nblk),
                  pl.BlockSpec((CONV_WIDTH, LANES), lambda c: (0, c))],
        out_specs=[col(0), col(0), pl.BlockSpec((CONV_WIDTH, LANES), lambda c: (0, c)),
                   pl.BlockSpec((1, LANES), lambda c: (0, c))],
        out_shape=[jax.ShapeDtypeStruct((s, ch), BF16)] * 2
        + [jax.ShapeDtypeStruct((CONV_WIDTH, ch), F32), jax.ShapeDtypeStruct((1, ch), F32)],
        operands=[duc, u0, proj, proj, conv_w],
        scratch_shapes=[pltpu.VMEM((s + 2 * CONV_PAD, LANES), F32)] * 2, rider=rider)


GATE_BLK = 512


def _gate_fwd(proj, bg, y_a, y_b, col0, name):
    s, d = y_a.shape
    tm = 256
    g0 = col0 // GATE_BLK
    nb = d // GATE_BLK

    def body(ga_ref, gb_ref, ba_ref, bb_ref, ya_ref, yb_ref, o_ref):
        g_a = jax.nn.sigmoid(ga_ref[...] + ba_ref[...])
        g_b = jax.nn.sigmoid(gb_ref[...] + bb_ref[...])
        o_ref[...] = (g_a * ya_ref[...] + g_b * yb_ref[...]).astype(BF16)

    act = pl.BlockSpec((tm, GATE_BLK), lambda i, j: (i, j))
    return pl.pallas_call(
        body, name=name, grid=(s // tm, nb),
        in_specs=[pl.BlockSpec((tm, GATE_BLK), lambda i, j: (i, g0 + j)),
                  pl.BlockSpec((tm, GATE_BLK), lambda i, j: (i, g0 + nb + j)),
                  pl.BlockSpec((None, 1, GATE_BLK), lambda i, j: (0, 0, j)),
                  pl.BlockSpec((None, 1, GATE_BLK), lambda i, j: (1, 0, j)), act, act],
        out_specs=act, out_shape=jax.ShapeDtypeStruct((s, d), BF16), compiler_params=_params(),
    )(proj, proj, bg, bg, y_a, y_b)


def _gate_bwd(d_mixed, proj, bg, y_a, y_b, col0, name, rider=None):
    s, d = y_a.shape
    tm = 256
    half = d // 2
    assert col0 % half == 0
    c0 = col0 // half

    def body(dm_ref, a0_ref, a1_ref, b0_ref, b1_ref, bias_ref, ya_ref, yb_ref, dgl_ref, dya_ref, dyb_ref, db_ref):
        dm = dm_ref[...]
        parts = []
        for br, (lo_ref, hi_ref, y_ref, dy_ref) in enumerate(((a0_ref, a1_ref, ya_ref, dya_ref),
                                                              (b0_ref, b1_ref, yb_ref, dyb_ref))):
            logits = jnp.concatenate([lo_ref[...], hi_ref[...]], axis=1)
            gate = jax.nn.sigmoid(logits + bias_ref[br])
            dy_ref[...] = (dm * gate).astype(BF16)
            dgl = dm * y_ref[...] * gate * (1.0 - gate)
            dgl_ref[:, br * d:(br + 1) * d] = dgl.astype(BF16)
            parts.append(jnp.sum(dgl, axis=0, keepdims=True))
        part = jnp.concatenate(parts, axis=0)
        first = pl.program_id(0) == 0

        @pl.when(first)
        def _():
            db_ref[...] = part

        @pl.when(jnp.logical_not(first))
        def _():
            db_ref[...] += part

    row = pl.BlockSpec((tm, d), lambda i: (i, 0))
    logit_blk = lambda k: pl.BlockSpec((tm, half), functools.partial(lambda i, k: (i, c0 + k), k=k))
    return _pallas(
        body, name=name, grid=(s // tm,),
        in_specs=[row, logit_blk(0), logit_blk(1), logit_blk(2), logit_blk(3),
                  pl.BlockSpec((2, 1, d), lambda i: (0, 0, 0)), row, row],
        out_specs=[pl.BlockSpec((tm, 2 * d), lambda i: (i, 0)), row, row, pl.BlockSpec((2, d), lambda i: (0, 0))],
        out_shape=[jax.ShapeDtypeStruct((s, 2 * d), BF16), jax.ShapeDtypeStruct((s, d), BF16),
                   jax.ShapeDtypeStruct((s, d), BF16), jax.ShapeDtypeStruct((2, d), F32)],
        operands=[d_mixed, proj, proj, proj, proj, bg, y_a, y_b], rider=rider)


def _ffn_in_swiglu(h2, w_blocked, name):
    s, k = h2.shape
    nblk, _, tn = w_blocked.shape
    ff = nblk // 2 * tn
    tm = 512

    def body(a_ref, wg_ref, wu_ref, g_ref, u_ref, act_ref):
        a = a_ref[...]
        gt = jnp.dot(a, wg_ref[...], preferred_element_type=F32)
        up = jnp.dot(a, wu_ref[...], preferred_element_type=F32)
        g_ref[...] = gt
        u_ref[...] = up
        act_ref[...] = (gt * jax.nn.sigmoid(gt) * up).astype(BF16)

    out = pl.BlockSpec((tm, tn), lambda i, j: (i, j))
    return pl.pallas_call(
        body, name=name, grid=(s // tm, nblk // 2),
        in_specs=[pl.BlockSpec((tm, k), lambda i, j: (i, 0)),
                  pl.BlockSpec((None, k, tn), lambda i, j: (j, 0, 0)),
                  pl.BlockSpec((None, k, tn), lambda i, j: (nblk // 2 + j, 0, 0))],
        out_specs=[out, out, out],
        out_shape=[jax.ShapeDtypeStruct((s, ff), F32), jax.ShapeDtypeStruct((s, ff), F32),
                   jax.ShapeDtypeStruct((s, ff), BF16)],
        compiler_params=_params(),
    )(h2, w_blocked, w_blocked)


def _swiglu_bwd(gate, up, d_act, name, rider=None):
    s, ff = gate.shape
    tm = 256

    def body(g_ref, u_ref, d_ref, o_ref):
        gt = g_ref[...]
        sg = jax.nn.sigmoid(gt)
        dv = d_ref[...]
        o_ref[:, 0:ff] = (dv * u_ref[...] * (sg * (1.0 + gt * (1.0 - sg)))).astype(BF16)
        o_ref[:, ff:2 * ff] = (dv * gt * sg).astype(BF16)

    row = pl.BlockSpec((tm, ff), lambda i: (i, 0))
    return _pallas(
        body, name=name, grid=(s // tm,), in_specs=[row, row, row],
        out_specs=pl.BlockSpec((tm, 2 * ff), lambda i: (i, 0)),
        out_shape=jax.ShapeDtypeStruct((s, 2 * ff), BF16), operands=[gate, up, d_act], rider=rider)


def _out_proj_rmsnorm(mixed, w_out, x, norm_w, name):
    s, k = mixed.shape
    d = w_out.shape[1]
    tm = 512

    def body(a_ref, w_ref, x_ref, nw_ref, x1_ref, h2_ref):
        x1 = x_ref[...] + jnp.dot(a_ref[...], w_ref[...], preferred_element_type=F32)
        x1_ref[...] = x1
        rstd = lax.rsqrt(jnp.mean(x1 * x1, axis=-1, keepdims=True) + EPS)
        h2_ref[...] = (x1 * rstd * nw_ref[...]).astype(BF16)

    row = pl.BlockSpec((tm, d), lambda i: (i, 0))
    return pl.pallas_call(
        body, name=name, grid=(s // tm,),
        in_specs=[pl.BlockSpec((tm, k), lambda i: (i, 0)), pl.BlockSpec((k, d), lambda i: (0, 0)), row,
                  pl.BlockSpec((1, d), lambda i: (0, 0))],
        out_specs=[row, row],
        out_shape=[jax.ShapeDtypeStruct((s, d), F32), jax.ShapeDtypeStruct((s, d), BF16)],
        compiler_params=_params(),
    )(mixed, w_out, x, norm_w)


def _ffn_out_loss(act, w_ffn_out, x1, target, name):
    s, k = act.shape
    d = w_ffn_out.shape[1]
    tm = 512

    def body(a_ref, w_ref, x1_ref, t_ref, dy_ref, dyb_ref, loss_ref, acc):
        y = x1_ref[...] + jnp.dot(a_ref[...], w_ref[...], preferred_element_type=F32)
        diff = y - t_ref[...]
        dy = diff * (1.0 / d)
        dy_ref[...] = dy
        dyb_ref[...] = dy.astype(BF16)
        part = jnp.sum((diff * diff).reshape(tm // 8, 8, d), axis=0)
        i = pl.program_id(0)

        @pl.when(i == 0)
        def _():
            acc[...] = part

        @pl.when(i > 0)
        def _():
            acc[...] += part

        @pl.when(i == pl.num_programs(0) - 1)
        def _():
            loss_ref[...] = (0.5 / d) * jnp.sum(jnp.sum(acc[...], axis=1, keepdims=True), axis=0, keepdims=True)

    row = pl.BlockSpec((tm, d), lambda i: (i, 0))
    return pl.pallas_call(
        body, name=name, grid=(s // tm,),
        in_specs=[pl.BlockSpec((tm, k), lambda i: (i, 0)), pl.BlockSpec((k, d), lambda i: (0, 0)), row, row],
        out_specs=[row, row, pl.BlockSpec((1, 1), lambda i: (0, 0))],
        out_shape=[jax.ShapeDtypeStruct((s, d), F32), jax.ShapeDtypeStruct((s, d), BF16),
                   jax.ShapeDtypeStruct((1, 1), F32)],
        scratch_shapes=[pltpu.VMEM((8, d), F32)], compiler_params=_params(),
    )(act, w_ffn_out, x1, target)


LATE_GATHER = ("w_o_attn", "w_pw_conv", "w_out", "w_ffn_in", "w_ffn_out")
EARLY_REDUCE = LATE_GATHER


def _blocks_by_half(g):
    if g.ndim == 2:
        g = g.reshape(N_CHIPS, g.shape[0] // N_CHIPS, g.shape[1])
    return g.reshape(N_CHIPS, 2, g.shape[1] // 2, g.shape[2])


def _forward_backward(x, pos_col, target, wts, late_bufs, pos_arr):
    wts = dict(wts)
    consts = _rope_consts()
    bd = consts[3]
    qw2 = jnp.tile(wts["q_norm_w"], (1, LANES // HEAD_DIM))
    kw2 = jnp.tile(wts["k_norm_w"], (1, LANES // HEAD_DIM))
    qkv_w = 3 * N_SLOT_HEADS * HEAD_DIM
    conv_col0 = 3 * qkv_w
    ch = wts["conv_w"].shape[1]
    gate_col0 = conv_col0 + 2 * ch

    h = _rmsnorm_fwd(x, wts["norm1_w"], "rms1_fwd")
    late = dict(zip(LATE_GATHER, late_bufs))
    quarter = late["w_ffn_in"].shape[2] // 4
    proj, (late["w_o_attn"], late["w_pw_conv"], late["w_out"], late["w_ffn_in"]) = _matmul(
        h, wts["w_in"], mode="nn", tm=512, tn=1920, tk=1024, out_dtype=F32, name="mm_proj", b_blocked=True,
        rider=_gather_ici_rider([late["w_o_attn"], late["w_pw_conv"], late["w_out"], late["w_ffn_in"]], [],
                                row_ranges=[None, None, None, (0, quarter)]))
    (qn, kn), (late["w_ffn_in"],) = _qk_fwd(
        proj, pos_col, qw2, kw2, consts, "qk_fwd",
        rider=_gather_ici_rider([late["w_ffn_in"]], [], row_ranges=[(quarter, 2 * quarter)]))
    (attn, lse, attn_b), (late["w_ffn_in"], late["w_ffn_out"]) = _attn_fwd(
        qn, kn, proj, "attn_fwd",
        rider=_gather_ici_rider([late["w_ffn_in"], late["w_ffn_out"]], [],
                                row_ranges=[(3 * quarter, quarter), None]))
    (u0, uc), late_bufs = _conv_fwd(proj, wts["conv_w"], wts["conv_b"], conv_col0, "conv_fwd",
                                    rider=_gather_forward_rider([late[n] for n in LATE_GATHER]))
    for n, buf in zip(LATE_GATHER, late_bufs):
        full = buf.reshape(N_CHIPS, -1, buf.shape[3])
        wts[n] = full.reshape(-1, full.shape[2]) if n in ROW_SHARDED else full
    y_a = _matmul(attn_b, wts["w_o_attn"], mode="nn", tm=1024, tn=256, tk=512, out_dtype=F32, name="mm_ya",
                  b_blocked=True)
    u3 = _ln_silu_fwd(uc, wts["conv_ln_w"], wts["conv_ln_b"], "ln_fwd")
    y_b = _matmul(u3, wts["w_pw_conv"], mode="nn", tm=1024, tn=256, tk=512, out_dtype=F32, name="mm_yb",
                  b_blocked=True)
    mixed = _gate_fwd(proj, wts["b_gate"], y_a, y_b, gate_col0, "gate_fwd")
    x1, h2 = _out_proj_rmsnorm(mixed, wts["w_out"], x, wts["norm2_w"], "mm_x1_rms2")
    gate, up, act = _ffn_in_swiglu(h2, wts["w_ffn_in"], "mm_gu_swiglu")
    dy, dy_b16, loss = _ffn_out_loss(act, wts["w_ffn_out"], x1, target, "mm_x2_loss")

    g = {}
    by_chip = {}

    def pair_add(n, blocks, received):
        return _add_own_half(blocks, received, pos_arr, f"grads_pair_add_{n}")

    d_act = _matmul(dy_b16, wts["w_ffn_out"], mode="nt", tm=512, tn=1408, tk=1024, out_dtype=F32, name="mm_dact")
    g_ffn_out = _blocks_by_half(
        _matmul(act, dy_b16, mode="tn", tm=1408, tn=1024, tk=2048, out_dtype=F32, name="mm_dwffnout"))
    dgu, (received,) = _swiglu_bwd(gate, up, d_act, "swiglu_bwd",
                                   rider=_pair_exchange_rider([g_ffn_out], halved=True))
    to_send, own = pair_add("w_ffn_out", g_ffn_out, received)
    dh2, (by_chip["w_ffn_out"],) = _matmul(
        dgu, wts["w_ffn_in"], mode="nt", tm=1024, tn=1024, tk=1408, out_dtype=F32, name="mm_dh2", b_blocked=True,
        rider=_chip_exchange_rider([to_send], [own]))
    g_ffn_in = _blocks_by_half(_matmul(h2, dgu, mode="tn", tm=512, tn=1408, tk=2048, out_dtype=F32,
                                       name="mm_dwffnin", out_blocked=N_CHIPS))
    dx1, dx1_b16, g["norm2_w"] = _rmsnorm_bwd(dh2, x1, wts["norm2_w"], dy, "rms2_bwd")
    d_mixed = _matmul(dx1_b16, wts["w_out"], mode="nt", tm=512, tn=1024, tk=1024, out_dtype=F32, name="mm_dmixed")
    g["w_out"] = _matmul(mixed, dx1_b16, mode="tn", tm=512, tn=1024, tk=2048, out_dtype=F32, name="mm_dwout")
    (dgl, dy_a, dy_b, g["b_gate"]), (received,) = _gate_bwd(
        d_mixed, proj, wts["b_gate"], y_a, y_b, gate_col0, "gate_bwd",
        rider=_pair_exchange_rider([g_ffn_in], halved=True))
    ffn_in_to_send, ffn_in_own = pair_add("w_ffn_in", g_ffn_in, received)
    dattn = _matmul(dy_a, wts["w_o_attn"], mode="nt", tm=1024, tn=512, tk=256, out_dtype=F32, name="mm_dattn",
                    b_blocked=True)
    g["w_o_attn"] = _matmul(attn_b, dy_a, mode="tn", tm=512, tn=256, tk=2048, out_dtype=F32, name="mm_dwo",
                            out_blocked=N_CHIPS)
    du3 = _matmul(dy_b, wts["w_pw_conv"], mode="nt", tm=1024, tn=512, tk=256, out_dtype=F32, name="mm_du3",
                  b_blocked=True)
    g["w_pw_conv"] = _matmul(u3, dy_b, mode="tn", tm=512, tn=256, tk=2048, out_dtype=F32, name="mm_dwpw",
                             out_blocked=N_CHIPS)
    duc, g["conv_ln_w"], g["conv_ln_b"] = _ln_silu_bwd(du3, uc, wts["conv_ln_w"], wts["conv_ln_b"], "ln_bwd")

    small3 = ("w_out", "w_o_attn", "w_pw_conv")
    g_small3 = [_blocks_by_half(g.pop(n)) for n in small3]
    (da, db, g["conv_w"], g["conv_b"]), received = _conv_bwd(
        duc, u0, proj, wts["conv_w"], conv_col0, "conv_bwd", rider=_pair_exchange_rider(g_small3, halved=True))
    sums3 = [pair_add(n, gb, rv) for n, gb, rv in zip(small3, g_small3, received)]
    (dqn, dkn, dv), (by_chip["w_ffn_in"],) = _attn_bwd(
        qn, kn, proj, dattn, attn, lse, bd, "attn_bwd",
        rider=_chip_exchange_rider([ffn_in_to_send], [ffn_in_own]))
    (dproj, dqw, dkw), exchanged3 = _qk_bwd(
        dqn, dkn, dv, da, db, dgl, proj, pos_col, qw2, kw2, consts, "qk_bwd",
        rider=_chip_exchange_rider([s[0] for s in sums3], [s[1] for s in sums3]))
    by_chip.update(zip(small3, exchanged3))
    halves = [_sum_chips(by_chip[n], pos_arr, f"grads_chip_sum_{n}") for n in EARLY_REDUCE]
    g["q_norm_w"] = dqw[:, :HEAD_DIM]
    g["k_norm_w"] = dkw[:, :HEAD_DIM]

    c = pos_arr[0]
    rh = h.shape[1] // 2
    h_sibling = lax.dynamic_slice_in_dim(h, (1 - c) * rh, rh, axis=1)
    h_own = lax.dynamic_slice_in_dim(h, c * rh, rh, axis=1)
    g_sibling, shards = _matmul(h_sibling, dproj, mode="tn", tm=rh, tn=1920, tk=2048, out_dtype=F32,
                                name="mm_dwin_sibling", out_blocked=N_CHIPS, rider=_pair_gather_rider(halves))
    reduced = dict(zip(EARLY_REDUCE, shards))
    g_own, from_sibling = _matmul(h_own, dproj, mode="tn", tm=rh, tn=1920, tk=2048, out_dtype=F32,
                                  name="mm_dwin_own", out_blocked=N_CHIPS,
                                  rider=_pair_exchange_rider([g_sibling], halved=False))
    to_send, own = _add_own_half(g_own, from_sibling[0], pos_arr, "grads_pair_add_w_in")
    *in_flight, token = _chip_exchange_start(to_send, own, "grads_w_in_exchange_start")
    in_flight = tuple(in_flight) + (token,)
    dh = _matmul(dproj, wts["w_in"], mode="nt", tm=1024, tn=1024, tk=1920, out_dtype=F32, name="mm_dh",
                 b_blocked=True, after=[token])
    grad_x, _, g["norm1_w"] = _rmsnorm_bwd(dh, x, wts["norm1_w"], dx1, "rms1_bwd")
    return loss, grad_x, g, reduced, in_flight


def _mesh_pos():
    return lax.axis_index("x"), lax.axis_index("y"), lax.axis_index("c")


def _other_chips(x, y):
    return [(1 - x, y), (x, 1 - y), (1 - x, 1 - y)]


def _cast_into_slot(shard, chip_arr, dtype, name):
    r, c = shard.shape
    tr = r // 2 if r % 32 == 0 else r

    def body(chip_ref, s_ref, o_ref):
        del chip_ref
        o_ref[...] = s_ref[...].astype(dtype)

    return pl.pallas_call(
        body, name=name,
        grid_spec=pltpu.PrefetchScalarGridSpec(
            num_scalar_prefetch=1, grid=(r // tr,),
            in_specs=[pl.BlockSpec((tr, c), lambda i, chip_ref: (i, 0))],
            out_specs=pl.BlockSpec((None, tr, c), lambda i, chip_ref: (chip_ref[0], i, 0))),
        out_shape=jax.ShapeDtypeStruct((N_CHIPS, r, c), dtype), compiler_params=_params(),
    )(chip_arr, shard)


def _allgather_inplace(big, small, name):
    nb, ns = len(big), len(small)
    n = nb + ns

    def body(*refs):
        bufs = refs[n:2 * n]
        send_sems, recv_sems, fsend_sems, frecv_sems = refs[2 * n:]
        x, y, c = _mesh_pos()
        me = 2 * x + y
        chips = _other_chips(x, y)

        def part(a, slot, half):
            return bufs[a].at[slot, half] if a < nb else bufs[a].at[slot]

        sends = []
        for a in range(n):
            for k, (px, py) in enumerate(chips):
                cp = pltpu.make_async_remote_copy(
                    src_ref=part(a, me, c), dst_ref=part(a, me, c), send_sem=send_sems.at[a, k],
                    recv_sem=recv_sems.at[a, k], device_id=(px, py, c), device_id_type=MESH)
                cp.start()
                sends.append(cp)
        for a in range(n):
            for k, (px, py) in enumerate(chips):
                slot = 2 * px + py
                pltpu.make_async_remote_copy(
                    src_ref=part(a, slot, c), dst_ref=part(a, slot, c), send_sem=send_sems.at[a, k],
                    recv_sem=recv_sems.at[a, k], device_id=(px, py, c), device_id_type=MESH).wait_recv()
                if a < nb:
                    fwd = pltpu.make_async_remote_copy(
                        src_ref=part(a, slot, c), dst_ref=part(a, slot, c), send_sem=fsend_sems.at[a, k],
                        recv_sem=frecv_sems.at[a, k], device_id=(x, y, 1 - c), device_id_type=MESH)
                    fwd.start()
                    sends.append(fwd)
        for a in range(nb):
            for k, (px, py) in enumerate(chips):
                slot = 2 * px + py
                pltpu.make_async_remote_copy(
                    src_ref=part(a, slot, 1 - c), dst_ref=part(a, slot, 1 - c), send_sem=fsend_sems.at[a, k],
                    recv_sem=frecv_sems.at[a, k], device_id=(x, y, 1 - c), device_id_type=MESH).wait_recv()
        for cp in sends:
            cp.wait_send()

    ops = list(big) + list(small)
    return pl.pallas_call(
        body, name=name, in_specs=[ANY] * n, out_specs=[ANY] * n,
        out_shape=[jax.ShapeDtypeStruct(o.shape, o.dtype) for o in ops],
        input_output_aliases={i: i for i in range(n)},
        scratch_shapes=[pltpu.SemaphoreType.DMA((n, 3)), pltpu.SemaphoreType.DMA((n, 3)),
                        pltpu.SemaphoreType.DMA((nb, 3)), pltpu.SemaphoreType.DMA((nb, 3))],
    )(*ops)


def _comm_call(rider, name):
    def body():
        pass

    return _pallas(body, name=name, grid=(1,), in_specs=[], out_specs=[], out_shape=[], operands=[],
                   rider=rider)[1]


def _gather_ici_rider(big, small, row_ranges=None):
    nb = len(big)
    n = nb + len(small)
    row_ranges = row_ranges or [None] * nb

    def copies(bufs, sems):
        x, y, c = _mesh_pos()
        me = 2 * x + y

        def part(a, slot):
            if a >= nb:
                return bufs[a].at[slot]
            if row_ranges[a] is None:
                return bufs[a].at[slot, c]
            return bufs[a].at[slot, c, pl.ds(*row_ranges[a])]
        out = []
        for a in range(n):
            for k, (px, py) in enumerate(_other_chips(x, y)):
                send = functools.partial(
                    pltpu.make_async_remote_copy,
                    src_ref=part(a, me), dst_ref=part(a, me), send_sem=sems[0].at[a, k],
                    recv_sem=sems[1].at[a, k], device_id=(px, py, c), device_id_type=MESH)
                recv = functools.partial(
                    pltpu.make_async_remote_copy,
                    src_ref=part(a, 2 * px + py), dst_ref=part(a, 2 * px + py), send_sem=sems[0].at[a, k],
                    recv_sem=sems[1].at[a, k], device_id=(px, py, c), device_id_type=MESH)
                out.append((send, recv))
        return out

    def start(r_in, r_out, sems):
        for send, _ in copies(r_out, sems):
            send().start()

    def wait(r_in, r_out, sems):
        cps = copies(r_out, sems)
        for _, recv in cps:
            recv().wait_recv()
        for send, _ in cps:
            send().wait_send()

    ops = list(big) + list(small)
    return _Rider(ops, [jax.ShapeDtypeStruct(o.shape, o.dtype) for o in ops], {i: i for i in range(n)},
                  [pltpu.SemaphoreType.DMA((n, 3)), pltpu.SemaphoreType.DMA((n, 3))], start, wait)


def _gather_forward_rider(big):
    n = len(big)

    def copies(bufs, sems):
        x, y, c = _mesh_pos()
        out = []
        for a in range(n):
            for k, (px, py) in enumerate(_other_chips(x, y)):
                slot = 2 * px + py
                send = functools.partial(
                    pltpu.make_async_remote_copy,
                    src_ref=bufs[a].at[slot, c], dst_ref=bufs[a].at[slot, c], send_sem=sems[0].at[a, k],
                    recv_sem=sems[1].at[a, k], device_id=(x, y, 1 - c), device_id_type=MESH)
                recv = functools.partial(
                    pltpu.make_async_remote_copy,
                    src_ref=bufs[a].at[slot, 1 - c], dst_ref=bufs[a].at[slot, 1 - c], send_sem=sems[0].at[a, k],
                    recv_sem=sems[1].at[a, k], device_id=(x, y, 1 - c), device_id_type=MESH)
                out.append((send, recv))
        return out

    def start(r_in, r_out, sems):
        for send, _ in copies(r_out, sems):
            send().start()

    def wait(r_in, r_out, sems):
        cps = copies(r_out, sems)
        for _, recv in cps:
            recv().wait_recv()
        for send, _ in cps:
            send().wait_send()

    return _Rider(big, [jax.ShapeDtypeStruct(o.shape, o.dtype) for o in big], {i: i for i in range(n)},
                  [pltpu.SemaphoreType.DMA((n, 3)), pltpu.SemaphoreType.DMA((n, 3))], start, wait)


def _pair_exchange_rider(gs, halved):
    n = len(gs)

    def copies(r_in, r_out, sems):
        x, y, c = _mesh_pos()
        return [pltpu.make_async_remote_copy(
            src_ref=r_in[a].at[:, 1 - c] if halved else r_in[a], dst_ref=r_out[a], send_sem=sems[0].at[a],
            recv_sem=sems[1].at[a], device_id=(x, y, 1 - c), device_id_type=MESH) for a in range(n)]

    def start(r_in, r_out, sems):
        for cp in copies(r_in, r_out, sems):
            cp.start()

    def wait(r_in, r_out, sems):
        for cp in copies(r_in, r_out, sems):
            cp.wait()

    return _Rider(gs, [jax.ShapeDtypeStruct((g.shape[0],) + g.shape[-2:], g.dtype) for g in gs], {},
                  [pltpu.SemaphoreType.DMA((n,)), pltpu.SemaphoreType.DMA((n,))], start, wait)


def _chip_exchange_rider(to_send, by_chip, row_range=None):
    n = len(to_send)

    def copies(r_in, r_out, sems):
        x, y, c = _mesh_pos()
        me = 2 * x + y
        rows = (lambda ref: ref) if row_range is None else (lambda ref: ref.at[pl.ds(*row_range)])
        out = []
        for a in range(n):
            for k, (px, py) in enumerate(_other_chips(x, y)):
                send = functools.partial(
                    pltpu.make_async_remote_copy,
                    src_ref=rows(r_in[a].at[2 * px + py]), dst_ref=rows(r_out[a].at[me]),
                    send_sem=sems[0].at[a, k], recv_sem=sems[1].at[a, k], device_id=(px, py, c),
                    device_id_type=MESH)
                recv = functools.partial(
                    pltpu.make_async_remote_copy,
                    src_ref=rows(r_in[a].at[me]), dst_ref=rows(r_out[a].at[2 * px + py]),
                    send_sem=sems[0].at[a, k], recv_sem=sems[1].at[a, k], device_id=(px, py, c),
                    device_id_type=MESH)
                out.append((send, recv))
        return out

    def start(r_in, r_out, sems):
        for send, _ in copies(r_in, r_out, sems):
            send().start()

    def wait(r_in, r_out, sems):
        cps = copies(r_in, r_out, sems)
        for _, recv in cps:
            recv().wait_recv()
        for send, _ in cps:
            send().wait_send()

    return _Rider(list(to_send) + list(by_chip), [jax.ShapeDtypeStruct(b.shape, b.dtype) for b in by_chip],
                  {n + i: i for i in range(n)},
                  [pltpu.SemaphoreType.DMA((n, 3)), pltpu.SemaphoreType.DMA((n, 3))], start, wait)


HBM = pl.BlockSpec(memory_space=pltpu.HBM)
SEM = pl.BlockSpec(memory_space=pltpu.SEMAPHORE)


def _chip_exchange_start(to_send, by_chip, name):
    def body(send_ref, buf_ref, send_sems, recv_sems, send_thru, buf_thru, token):
        x, y, c = _mesh_pos()
        me = 2 * x + y
        for k, (px, py) in enumerate(_other_chips(x, y)):
            pltpu.make_async_remote_copy(
                src_ref=send_ref.at[2 * px + py], dst_ref=buf_ref.at[me], send_sem=send_sems.at[k],
                recv_sem=recv_sems.at[k], device_id=(px, py, c), device_id_type=MESH).start()
        token[...] = jnp.zeros_like(token)

    return pl.pallas_call(
        body, name=name,
        out_shape=(pltpu.SemaphoreType.DMA((3,)), pltpu.SemaphoreType.DMA((3,)),
                   pltpu.HBM(to_send.shape, to_send.dtype), pltpu.HBM(by_chip.shape, by_chip.dtype),
                   jax.ShapeDtypeStruct((8, LANES), F32)),
        in_specs=(HBM, HBM), out_specs=(SEM, SEM, HBM, HBM, pl.BlockSpec(memory_space=pltpu.VMEM)),
        input_output_aliases={0: 2, 1: 3},
        compiler_params=pltpu.CompilerParams(has_side_effects=pltpu.SideEffectType.DATAFLOW_SIDE_EFFECTING),
    )(pltpu.with_memory_space_constraint(to_send, pltpu.HBM), pltpu.with_memory_space_constraint(by_chip, pltpu.HBM))


def _chip_exchange_wait(send_sems, recv_sems, send_thru, buf_thru, after, name):
    n_after = len(after)

    def body(send_ref, buf_ref, send_sems, recv_sems, *rest):
        x, y, c = _mesh_pos()
        me = 2 * x + y
        for k, (px, py) in enumerate(_other_chips(x, y)):
            cp = pltpu.make_async_remote_copy(
                src_ref=send_ref.at[me], dst_ref=buf_ref.at[2 * px + py], send_sem=send_sems.at[k],
                recv_sem=recv_sems.at[k], device_id=(px, py, c), device_id_type=MESH)
            cp.wait_send()
            cp.wait_recv()

    return pl.pallas_call(
        body, name=name,
        out_shape=(pltpu.HBM(send_thru.shape, send_thru.dtype), pltpu.HBM(buf_thru.shape, buf_thru.dtype)),
        in_specs=(HBM, HBM, SEM, SEM) + (ANY,) * n_after, out_specs=(HBM, HBM), input_output_aliases={0: 0, 1: 1},
        compiler_params=pltpu.CompilerParams(has_side_effects=pltpu.SideEffectType.DATAFLOW_SIDE_EFFECTING),
    )(send_thru, buf_thru, send_sems, recv_sems, *after)[1]


def _pair_gather_rider(bufs):
    n = len(bufs)

    def copies(r_out, sems):
        x, y, c = _mesh_pos()
        out = []
        for a in range(n):
            send = functools.partial(
                    pltpu.make_async_remote_copy,
                src_ref=r_out[a].at[c], dst_ref=r_out[a].at[c], send_sem=sems[0].at[a],
                recv_sem=sems[1].at[a], device_id=(x, y, 1 - c), device_id_type=MESH)
            recv = functools.partial(
                    pltpu.make_async_remote_copy,
                src_ref=r_out[a].at[1 - c], dst_ref=r_out[a].at[1 - c], send_sem=sems[0].at[a],
                recv_sem=sems[1].at[a], device_id=(x, y, 1 - c), device_id_type=MESH)
            out.append((send, recv))
        return out

    def start(r_in, r_out, sems):
        for send, _ in copies(r_out, sems):
            send().start()

    def wait(r_in, r_out, sems):
        cps = copies(r_out, sems)
        for _, recv in cps:
            recv().wait_recv()
        for send, _ in cps:
            send().wait_send()

    return _Rider(bufs, [jax.ShapeDtypeStruct(b.shape, b.dtype) for b in bufs], {i: i for i in range(n)},
                  [pltpu.SemaphoreType.DMA((n,)), pltpu.SemaphoreType.DMA((n,))], start, wait)


def _add_own_half(g, recv, pos_arr, name):
    nb, rh, cols = g.shape[0], g.shape[-2], g.shape[-1]

    def body(pos_ref, g_ref, r_ref, send_ref, own_ref):
        s = (g_ref[...] + r_ref[...]).astype(BF16)
        send_ref[...] = s

        @pl.when(pl.program_id(0) == pos_ref[1])
        def _():
            own_ref[...] = s

    blk = pl.BlockSpec((None, rh, cols), lambda j, pos_ref: (j, 0, 0))
    g_spec = blk if g.ndim == 3 else pl.BlockSpec((None, None, rh, cols),
                                                   lambda j, pos_ref: (j, pos_ref[0], 0, 0))
    shape = jax.ShapeDtypeStruct((nb, rh, cols), BF16)
    return pl.pallas_call(
        body, name=name,
        grid_spec=pltpu.PrefetchScalarGridSpec(
            num_scalar_prefetch=1, grid=(nb,), in_specs=[g_spec, blk],
            out_specs=[blk, pl.BlockSpec((None, rh, cols), lambda j, pos_ref: (pos_ref[1], 0, 0))]),
        out_shape=[shape, shape], compiler_params=_params(),
    )(pos_arr, g, recv)


def _sum_chips(gath, pos_arr, name):
    nb, rh, cols = gath.shape

    def body(pos_ref, a_ref, b_ref, c_ref, d_ref, o_ref):
        del pos_ref
        o_ref[...] = ((a_ref[...].astype(F32) + b_ref[...].astype(F32)) + c_ref[...].astype(F32)) \
            + d_ref[...].astype(F32)

    tr = rh // 2 if (rh // 2) % 16 == 0 else rh
    specs = [pl.BlockSpec((None, tr, cols), functools.partial(lambda i, pos_ref, j: (j, i, 0), j=j))
             for j in range(nb)]
    return pl.pallas_call(
        body, name=name,
        grid_spec=pltpu.PrefetchScalarGridSpec(
            num_scalar_prefetch=1, grid=(rh // tr,), in_specs=specs,
            out_specs=pl.BlockSpec((None, tr, cols), lambda i, pos_ref: (pos_ref[0], i, 0))),
        out_shape=jax.ShapeDtypeStruct((2, rh, cols), F32), compiler_params=_params(),
    )(pos_arr, gath, gath, gath, gath)


def _small_allreduce(v, name, rider=None, after=()):
    n = v.shape[0]
    n_dev = 8

    def body(v_ref, o_ref, buf, send_sems, recv_sems):
        x, y, c = _mesh_pos()
        me = 4 * x + 2 * y + c
        buf[me] = v_ref[...]
        sends = []
        peers = []
        for r in range(1, n_dev):
            px = 1 - x if r & 4 else x
            py = 1 - y if r & 2 else y
            pc = 1 - c if r & 1 else c
            peers.append((px, py, pc))
            cp = pltpu.make_async_remote_copy(
                src_ref=v_ref, dst_ref=buf.at[me], send_sem=send_sems.at[r - 1],
                recv_sem=recv_sems.at[r - 1], device_id=(px, py, pc), device_id_type=MESH)
            cp.start()
            sends.append(cp)
        for r, (px, py, pc) in enumerate(peers):
            pltpu.make_async_remote_copy(
                src_ref=v_ref, dst_ref=buf.at[4 * px + 2 * py + pc], send_sem=send_sems.at[r],
                recv_sem=recv_sems.at[r], device_id=(px, py, pc), device_id_type=MESH).wait_recv()
        for cp in sends:
            cp.wait_send()
        acc = buf[0]
        for i in range(1, n_dev):
            acc = acc + buf[i]
        o_ref[...] = acc

    whole = pl.BlockSpec(v.shape, lambda i: (0, 0))
    return _pallas(
        body, name=name, grid=(1,), in_specs=[whole], out_specs=whole,
        out_shape=jax.ShapeDtypeStruct(v.shape, v.dtype), operands=[v],
        scratch_shapes=[pltpu.VMEM((n_dev, n, LANES), F32), pltpu.SemaphoreType.DMA((n_dev - 1,)),
                        pltpu.SemaphoreType.DMA((n_dev - 1,))],
        rider=rider, after=after)


def _adamw_math(w, g, m, v):
    m = ADAM_B1 * m + (1.0 - ADAM_B1) * g
    v = ADAM_B2 * v + (1.0 - ADAM_B2) * (g * g)
    m_hat = m / (1.0 - ADAM_B1 ** ADAM_STEP)
    v_hat = v / (1.0 - ADAM_B2 ** ADAM_STEP)
    delta = -ADAM_LR * (m_hat / (jnp.sqrt(v_hat) + ADAM_EPS) + ADAM_WD * w)
    return delta, m, v


def _adamw(w, g, m, v, name, after=()):
    r, c = w.shape
    tr = 128 if r % 128 == 0 else 64
    assert r % tr == 0

    def body(w_ref, g_ref, m_ref, v_ref, go_ref, d_ref, mo_ref, vo_ref):
        gv = g_ref[...]
        d, mn, vn = _adamw_math(w_ref[...], gv, m_ref[...], v_ref[...])
        go_ref[...] = gv
        d_ref[...] = d
        mo_ref[...] = mn
        vo_ref[...] = vn

    blk = pl.BlockSpec((tr, c), lambda i: (i, 0))
    return _pallas(body, name=name, grid=(r // tr,), in_specs=[blk] * 4, out_specs=[blk] * 4,
                   out_shape=[jax.ShapeDtypeStruct((r, c), F32)] * 4, operands=[w, g, m, v], after=after)


def _adamw_small(ws, gs, ms, vs, name):
    n = len(ws)

    def body(*refs):
        w_r, g_r, m_r, v_r = refs[:n], refs[n:2 * n], refs[2 * n:3 * n], refs[3 * n:4 * n]
        d_o, m_o, v_o = refs[4 * n:5 * n], refs[5 * n:6 * n], refs[6 * n:7 * n]
        for i in range(n):
            d, mn, vn = _adamw_math(w_r[i][...], g_r[i][...], m_r[i][...], v_r[i][...])
            d_o[i][...] = d
            m_o[i][...] = mn
            v_o[i][...] = vn

    specs = [pl.BlockSpec(w.shape, lambda i: (0, 0)) for w in ws]
    shapes = [jax.ShapeDtypeStruct(w.shape, F32) for w in ws]
    outs = pl.pallas_call(
        body, name=name, grid=(1,), in_specs=specs * 4, out_specs=specs * 3, out_shape=shapes * 3,
        compiler_params=_params(),
    )(*ws, *gs, *ms, *vs)
    return outs[:n], outs[n:2 * n], outs[2 * n:]


BIG = ("w_in", "w_o_attn", "w_pw_conv", "w_out", "w_ffn_in", "w_ffn_out")
ROW_SHARDED = ("w_out", "w_ffn_out")
SMALL = ("norm1_w", "b_gate", "q_norm_w", "k_norm_w", "conv_w", "conv_b", "conv_ln_w", "conv_ln_b", "norm2_w")
ORDER = ("norm1_w", "w_in", "b_gate", "q_norm_w", "k_norm_w", "w_o_attn", "conv_w", "conv_b", "conv_ln_w",
         "conv_ln_b", "w_pw_conv", "w_out", "norm2_w", "w_ffn_in", "w_ffn_out")
PACK_TILE = 8 * LANES


def _pack_small(parts):
    rows = []
    for p in parts:
        flat = p.reshape(-1)
        pad = (-flat.shape[0]) % PACK_TILE
        rows.append(jnp.pad(flat, (0, pad)).reshape(-1, LANES))
    return jnp.concatenate(rows, axis=0)


def _unpack_small(packed, shapes):
    out, row = [], 0
    for shp in shapes:
        size = int(np.prod(shp))
        nrow = -(-size // PACK_TILE) * (PACK_TILE // LANES)
        out.append(packed[row:row + nrow].reshape(-1)[:size].reshape(shp))
        row += nrow
    return out


def kernel(x, positions, norm1_w, w_in, b_gate, q_norm_w, k_norm_w, w_o_attn, conv_w, conv_b, conv_ln_w, conv_ln_b, w_pw_conv, w_out, norm2_w, w_ffn_in, w_ffn_out, loss_target, m_norm1_w, m_w_in, m_b_gate, m_q_norm_w, m_k_norm_w, m_w_o_attn, m_conv_w, m_conv_b, m_conv_ln_w, m_conv_ln_b, m_w_pw_conv, m_w_out, m_norm2_w, m_w_ffn_in, m_w_ffn_out, v_norm1_w, v_w_in, v_b_gate, v_q_norm_w, v_k_norm_w, v_w_o_attn, v_conv_w, v_conv_b, v_conv_ln_w, v_conv_ln_b, v_w_pw_conv, v_w_out, v_norm2_w, v_w_ffn_in, v_w_ffn_out):
    w = dict(norm1_w=norm1_w, w_in=w_in, b_gate=b_gate, q_norm_w=q_norm_w, k_norm_w=k_norm_w, w_o_attn=w_o_attn,
             conv_w=conv_w, conv_b=conv_b, conv_ln_w=conv_ln_w, conv_ln_b=conv_ln_b, w_pw_conv=w_pw_conv,
             w_out=w_out, norm2_w=norm2_w, w_ffn_in=w_ffn_in, w_ffn_out=w_ffn_out)
    m = dict(norm1_w=m_norm1_w, w_in=m_w_in, b_gate=m_b_gate, q_norm_w=m_q_norm_w, k_norm_w=m_k_norm_w,
             w_o_attn=m_w_o_attn, conv_w=m_conv_w, conv_b=m_conv_b, conv_ln_w=m_conv_ln_w,
             conv_ln_b=m_conv_ln_b, w_pw_conv=m_w_pw_conv, w_out=m_w_out, norm2_w=m_norm2_w,
             w_ffn_in=m_w_ffn_in, w_ffn_out=m_w_ffn_out)
    v = dict(norm1_w=v_norm1_w, w_in=v_w_in, b_gate=v_b_gate, q_norm_w=v_q_norm_w, k_norm_w=v_k_norm_w,
             w_o_attn=v_w_o_attn, conv_w=v_conv_w, conv_b=v_conv_b, conv_ln_w=v_conv_ln_w,
             conv_ln_b=v_conv_ln_b, w_pw_conv=v_w_pw_conv, w_out=v_w_out, norm2_w=v_norm2_w,
             w_ffn_in=v_w_ffn_in, w_ffn_out=v_w_ffn_out)
    cx, cy, cc = _mesh_pos()
    chip = 2 * cx + cy

    chip_arr = chip.reshape(1).astype(jnp.int32)
    pos_arr = jnp.stack([cc, chip]).astype(jnp.int32)
    bufs = {}
    for n in BIG:
        buf = _cast_into_slot(w[n][0], chip_arr, BF16, f"cast_{n}")
        bufs[n] = buf.reshape(N_CHIPS, 2, buf.shape[1] // 2, buf.shape[2])
    small_bufs = [_cast_into_slot(w[n][0], chip_arr, F32, f"slot_{n}") for n in ("conv_w", "b_gate")]
    w_in_buf, conv_w_buf, b_gate_buf = _allgather_inplace([bufs["w_in"]], small_bufs, "allgather_w_in")
    wts = dict(w_in=w_in_buf.reshape(N_CHIPS, -1, w_in_buf.shape[3]),
               conv_w=conv_w_buf.transpose(1, 0, 2).reshape(CONV_WIDTH, -1),
               b_gate=b_gate_buf.transpose(1, 0, 2).reshape(2, 1, -1),
               norm1_w=norm1_w, q_norm_w=q_norm_w, k_norm_w=k_norm_w, conv_b=conv_b, conv_ln_w=conv_ln_w,
               conv_ln_b=conv_ln_b, norm2_w=norm2_w)

    loss, grad_x, g, reduced, w_in_in_flight = _forward_backward(
        x[0], positions.reshape(-1, 1), loss_target[0], wts, [bufs[n] for n in LATE_GATHER], pos_arr)
    grads = {n: b.reshape(-1, b.shape[2]) for n, b in reduced.items()}

    *w_in_in_flight, started = w_in_in_flight
    delta, new_m, new_v = {}, {}, {}
    for n in EARLY_REDUCE:
        grads[n], delta[n], new_m[n], new_v[n] = _adamw(w[n][0], grads[n], m[n][0], v[n][0], f"adamw_{n}",
                                                        after=[started])
    small_parts = [loss] + [g[n] for n in SMALL]
    small_shapes = [p.shape for p in small_parts]
    summed = _small_allreduce(_pack_small(small_parts), "small_allreduce",
                              after=[delta[n] for n in EARLY_REDUCE])
    reduced = _unpack_small(summed, small_shapes)
    loss_total = reduced[0].reshape(())
    for n, r in zip(SMALL, reduced[1:]):
        grads[n] = r
    ch_shard = conv_w.shape[2]
    grads["conv_w"] = lax.dynamic_slice_in_dim(grads["conv_w"], chip * ch_shard, ch_shard, axis=1)
    d_shard = b_gate.shape[2]
    grads["b_gate"] = lax.dynamic_slice_in_dim(grads["b_gate"], chip * d_shard, d_shard, axis=1)

    by_chip_w_in = _chip_exchange_wait(*w_in_in_flight, after=[delta[n] for n in EARLY_REDUCE] + [summed],
                                       name="grads_w_in_exchange_wait")
    half_w_in = _sum_chips(by_chip_w_in, pos_arr, "grads_chip_sum_w_in")
    (shard_w_in,) = _comm_call(_pair_gather_rider([half_w_in]), "grads_pair_gather_w_in")
    grads["w_in"], delta["w_in"], new_m["w_in"], new_v["w_in"] = _adamw(
        w["w_in"][0], shard_w_in.reshape(-1, shard_w_in.shape[2]), m["w_in"][0], v["w_in"][0], "adamw_w_in")
    flat2 = lambda a: a.reshape(-1, a.shape[-1])
    d_s, m_s, v_s = _adamw_small([flat2(w[n]) for n in SMALL], [flat2(grads[n]) for n in SMALL],
                                 [flat2(m[n]) for n in SMALL], [flat2(v[n]) for n in SMALL], "adamw_small")
    for i, n in enumerate(SMALL):
        delta[n], new_m[n], new_v[n] = d_s[i], m_s[i], v_s[i]

    shaped = lambda d, n: d[n].reshape(w[n].shape)
    return (loss_total, grad_x[None], *[shaped(grads, n) for n in ORDER], *[shaped(delta, n) for n in ORDER],
            *[shaped(new_m, n) for n in ORDER], *[shaped(new_v, n) for n in ORDER])
```

```python
import functools

import numpy as np
import jax
import jax.numpy as jnp
from jax import lax
from jax.experimental import pallas as pl
from jax.experimental.pallas import tpu as pltpu

F32 = jnp.float32
BF16 = jnp.bfloat16
MESH = pl.DeviceIdType.MESH
ANY = pl.BlockSpec(memory_space=pl.ANY)

HEAD_DIM = 64
N_SLOT_HEADS = 8
DILATIONS = (1, 4, 16)
HALF_SPAN = 64
ROPE_THETA = 500000.0
ROT_DIM = 16
CONV_WIDTH = 31
EPS = 1e-6
NEG_INF = -1e30
ADAM_LR, ADAM_B1, ADAM_B2, ADAM_EPS, ADAM_WD, ADAM_STEP = 0.001, 0.9, 0.999, 1e-08, 0.01, 10

LANES = 128
QBLK = 128
KWIN = QBLK + 2 * HALF_SPAN
VMEM_LIMIT = 48 * 1024 * 1024
N_CHIPS = 4


def _params(**kw):
    return pltpu.CompilerParams(vmem_limit_bytes=VMEM_LIMIT, **kw)


PIN_TO_HBM_BYTES = 1 << 20


def _pin(out_shape):
    def one(s):
        if isinstance(s, jax.ShapeDtypeStruct) and np.prod(s.shape) * jnp.dtype(s.dtype).itemsize >= PIN_TO_HBM_BYTES:
            return pltpu.HBM(tuple(s.shape), s.dtype)
        return s

    return [one(s) for s in out_shape] if isinstance(out_shape, (list, tuple)) else one(out_shape)


def _pinned_call(body, *, out_shape, **kw):
    return pl.pallas_call(body, out_shape=_pin(out_shape), **kw)


class _Rider:
    def __init__(self, operands, out_shapes, aliases, scratch, start, wait):
        self.operands, self.out_shapes, self.aliases = list(operands), list(out_shapes), dict(aliases)
        self.scratch, self.start, self.wait = list(scratch), start, wait


def _pallas(body, *, name, grid, in_specs, out_specs, out_shape, operands, scratch_shapes=(), aliases=None,
            rider=None, after=()):
    single = not isinstance(out_specs, (list, tuple))
    out_specs_l = [out_specs] if single else list(out_specs)
    out_shape_l = _pin([out_shape] if single else list(out_shape))
    aliases = dict(aliases or {})
    if rider is None:
        n_main = len(in_specs)

        def ordered(*refs):
            body(*refs[:n_main], *refs[n_main + len(after):])

        res = pl.pallas_call(
            ordered if after else body, name=name, grid=grid, in_specs=list(in_specs) + [ANY] * len(after),
            out_specs=out_specs_l, out_shape=out_shape_l, scratch_shapes=list(scratch_shapes),
            input_output_aliases=aliases, compiler_params=_params(),
        )(*operands, *after)
        return res[0] if single else res
    assert not after
    n_in, n_rin = len(in_specs), len(rider.operands)
    n_out, n_rout = len(out_specs_l), len(rider.out_shapes)
    n_sc = len(scratch_shapes)

    def wrapped(*refs):
        main_in, r_in = refs[:n_in], refs[n_in:n_in + n_rin]
        o0 = n_in + n_rin
        main_out, r_out = refs[o0:o0 + n_out], refs[o0 + n_out:o0 + n_out + n_rout]
        s0 = o0 + n_out + n_rout
        main_sc, r_sc = refs[s0:s0 + n_sc], refs[s0 + n_sc:]
        ids = [pl.program_id(d) for d in range(len(grid))]
        first = functools.reduce(jnp.logical_and, [i == 0 for i in ids])
        last = functools.reduce(jnp.logical_and, [i == n - 1 for i, n in zip(ids, grid)])

        @pl.when(first)
        def _():
            rider.start(r_in, r_out, r_sc)

        body(*main_in, *main_out, *main_sc)

        @pl.when(last)
        def _():
            rider.wait(r_in, r_out, r_sc)

    for src, dst in rider.aliases.items():
        aliases[n_in + src] = n_out + dst
    res = pl.pallas_call(
        wrapped, name=name, grid=grid, in_specs=list(in_specs) + [ANY] * n_rin,
        out_specs=out_specs_l + [ANY] * n_rout, out_shape=out_shape_l + _pin(rider.out_shapes),
        scratch_shapes=list(scratch_shapes) + rider.scratch, input_output_aliases=aliases,
        compiler_params=_params(),
    )(*operands, *rider.operands)
    main = res[:n_out]
    return (main[0] if single else main), res[n_out:]


def _matmul(a, b, *, mode, tm, tn, tk, out_dtype, name, b_blocked=False,
            out_blocked=None, rider=None, after=()):
    a_shape = a.shape
    if mode == "nn":
        m_dim, k_dim = a_shape
        n_dim = b.shape[0] * b.shape[2] if b_blocked else b.shape[1]
        rows, cols, red = m_dim, n_dim, k_dim
    elif mode == "nt":
        m_dim, n_dim = a_shape
        k_dim = b.shape[1] if b_blocked else b.shape[0]
        rows, cols, red = m_dim, k_dim, n_dim
    else:
        m_dim, k_dim = a_shape
        n_dim = b.shape[1]
        rows, cols, red = k_dim, n_dim, m_dim
    assert rows % tm == 0 and cols % tn == 0 and red % tk == 0, (name, rows, cols, red)
    ni, nj, nk = rows // tm, cols // tn, red // tk

    if mode == "nn":
        a_spec = pl.BlockSpec((tm, tk), lambda i, j, k: (i, k))
        if b_blocked:
            per = b.shape[2] // tn
            b_spec = pl.BlockSpec((None, tk, tn), lambda i, j, k: (j // per, k, j % per))
        else:
            b_spec = pl.BlockSpec((tk, tn), lambda i, j, k: (k, j))
        dims = (((1,), (0,)), ((), ()))
    elif mode == "nt":
        a_spec = pl.BlockSpec((tm, tk), lambda i, j, k: (i, k))
        if b_blocked:
            per = b.shape[2] // tk
            b_spec = pl.BlockSpec((None, tn, tk), lambda i, j, k: (k // per, j, k % per))
        else:
            b_spec = pl.BlockSpec((tn, tk), lambda i, j, k: (j, k))
        dims = (((1,), (1,)), ((), ()))
    else:
        a_spec = pl.BlockSpec((tk, tm), lambda i, j, k: (k, i))
        b_spec = pl.BlockSpec((tk, tn), lambda i, j, k: (k, j))
        dims = (((0,), (0,)), ((), ()))

    if out_blocked:
        per_o = (cols // out_blocked) // tn
        out_spec = pl.BlockSpec((None, tm, tn), lambda i, j, k: (j // per_o, i, j % per_o))
        out_shape = jax.ShapeDtypeStruct((out_blocked, rows, cols // out_blocked), out_dtype)
    else:
        out_spec = pl.BlockSpec((tm, tn), lambda i, j, k: (i, j))
        out_shape = jax.ShapeDtypeStruct((rows, cols), out_dtype)

    def body(a_ref, b_ref, o_ref, *acc):
        prod = lax.dot_general(a_ref[...], b_ref[...], dims, preferred_element_type=F32)
        if nk == 1:
            o_ref[...] = prod.astype(out_dtype)
        else:
            acc_ref, = acc
            k = pl.program_id(2)

            @pl.when(k == 0)
            def _():
                acc_ref[...] = prod

            @pl.when(k > 0)
            def _():
                acc_ref[...] += prod

            @pl.when(k == nk - 1)
            def _():
                o_ref[...] = acc_ref[...].astype(out_dtype)

    scratch = [pltpu.VMEM((tm, tn), F32)] if nk > 1 else []
    return _pallas(body, name=name, grid=(ni, nj, nk), in_specs=[a_spec, b_spec], out_specs=out_spec,
                   out_shape=out_shape, operands=[a, b], scratch_shapes=scratch, rider=rider, after=after)


def _rmsnorm_fwd(x, w, name):
    s, d = x.shape
    tm = 256

    def body(x_ref, w_ref, o_ref):
        xv = x_ref[...]
        rstd = lax.rsqrt(jnp.mean(xv * xv, axis=-1, keepdims=True) + EPS)
        o_ref[...] = (xv * rstd * w_ref[...]).astype(BF16)

    return _pinned_call(
        body, name=name, grid=(s // tm,),
        in_specs=[pl.BlockSpec((tm, d), lambda i: (i, 0)), pl.BlockSpec((1, d), lambda i: (0, 0))],
        out_specs=pl.BlockSpec((tm, d), lambda i: (i, 0)),
        out_shape=jax.ShapeDtypeStruct((s, d), BF16), compiler_params=_params(),
    )(x, w)


def _rmsnorm_bwd(dh, x, w, dres, name, rider=None):
    s, d = x.shape
    tm = 256

    def body(dh_ref, x_ref, w_ref, dres_ref, dx_ref, dxb_ref, dw_ref):
        xv = x_ref[...]
        rstd = lax.rsqrt(jnp.mean(xv * xv, axis=-1, keepdims=True) + EPS)
        xhat = xv * rstd
        dhv = dh_ref[...]
        g = dhv * w_ref[...]
        dx = rstd * (g - xhat * jnp.mean(g * xhat, axis=-1, keepdims=True)) + dres_ref[...]
        dx_ref[...] = dx
        dxb_ref[...] = dx.astype(BF16)
        part = jnp.sum(dhv * xhat, axis=0, keepdims=True)

        @pl.when(pl.program_id(0) == 0)
        def _():
            dw_ref[...] = part

        @pl.when(pl.program_id(0) > 0)
        def _():
            dw_ref[...] += part

    row = pl.BlockSpec((tm, d), lambda i: (i, 0))
    vec = pl.BlockSpec((1, d), lambda i: (0, 0))
    return _pallas(
        body, name=name, grid=(s // tm,), in_specs=[row, row, vec, row], out_specs=[row, row, vec],
        out_shape=[jax.ShapeDtypeStruct((s, d), F32), jax.ShapeDtypeStruct((s, d), BF16),
                   jax.ShapeDtypeStruct((1, d), F32)],
        operands=[dh, x, w, dres], rider=rider)


def _rope_consts():
    lane = np.arange(LANES)
    in_head = lane % HEAD_DIM
    inv_freq = ROPE_THETA ** (-jnp.arange(0, ROT_DIM, 2, dtype=F32) / ROT_DIM)
    invf = jnp.where(jnp.asarray(in_head < ROT_DIM), jnp.tile(inv_freq, LANES // (ROT_DIM // 2)), 0.0)
    m_a = np.where(in_head < ROT_DIM // 2, -1.0, 0.0).astype(np.float32)
    m_b = np.where((in_head >= ROT_DIM // 2) & (in_head < ROT_DIM), 1.0, 0.0).astype(np.float32)
    block_diag = (lane[:, None] // HEAD_DIM == lane[None, :] // HEAD_DIM).astype(np.float32)
    return (invf.reshape(1, LANES).astype(F32), jnp.asarray(m_a).reshape(1, LANES),
            jnp.asarray(m_b).reshape(1, LANES), jnp.asarray(block_diag, dtype=BF16))


def _head_sums(v, bd):
    hi = v.astype(BF16)
    lo = (v - hi.astype(F32)).astype(BF16)
    return jnp.dot(hi, bd, preferred_element_type=F32) + jnp.dot(lo, bd, preferred_element_type=F32)


def _qk_fwd(proj, pos_col, qw2, kw2, consts, name, rider=None):
    s = proj.shape[0]
    width = 3 * N_SLOT_HEADS * HEAD_DIM
    tm = 128
    invf, m_a, m_b, bd = consts
    scale = HEAD_DIM ** -0.5

    def body(q_ref, k_ref, pos_ref, qw_ref, kw_ref, invf_ref, ma_ref, mb_ref, bd_ref, qo_ref, ko_ref):
        ang = pos_ref[...].astype(F32) * invf_ref[...]
        cos = jnp.cos(ang)
        sin = jnp.sin(ang)
        s_a = sin * ma_ref[...]
        s_b = sin * mb_ref[...]
        bdv = bd_ref[...]
        for src, w_ref, dst, sc in ((q_ref, qw_ref, qo_ref, scale), (k_ref, kw_ref, ko_ref, 1.0)):
            for cb in range(width // LANES):
                cols = slice(cb * LANES, (cb + 1) * LANES)
                t = src[:, cols]
                rstd = lax.rsqrt(_head_sums(t * t, bdv) * (1.0 / HEAD_DIM) + EPS)
                y = t * rstd * w_ref[...]
                r = y * cos + pltpu.roll(y, LANES - 8, axis=1) * s_a + pltpu.roll(y, 8, axis=1) * s_b
                dst[:, cols] = r * sc if sc != 1.0 else r

    vec = pl.BlockSpec((1, LANES), lambda i: (0, 0))
    return _pallas(
        body, name=name, grid=(s // tm,),
        in_specs=[pl.BlockSpec((tm, width), lambda i: (i, 0)), pl.BlockSpec((tm, width), lambda i: (i, 1)),
                  pl.BlockSpec((tm, 1), lambda i: (i, 0)), vec, vec, vec, vec, vec,
                  pl.BlockSpec((LANES, LANES), lambda i: (0, 0))],
        out_specs=[pl.BlockSpec((tm, width), lambda i: (i, 0))] * 2,
        out_shape=[jax.ShapeDtypeStruct((s, width), F32)] * 2,
        operands=[proj, proj, pos_col, qw2, kw2, invf, m_a, m_b, bd], rider=rider)


def _qk_bwd(dqn, dkn, dv, da, db, dgl, proj, pos_col, qw2, kw2, consts, name, rider=None):
    s = proj.shape[0]
    width = 3 * N_SLOT_HEADS * HEAD_DIM
    ch = da.shape[1]
    gate_w = dgl.shape[1]
    out_w = 3 * width + 2 * ch + gate_w
    assert out_w == proj.shape[1]
    tm = 128
    invf, m_a, m_b, bd = consts
    scale = HEAD_DIM ** -0.5

    def body(dq_ref, dk_ref, dv_ref, da_ref, db_ref, dgl_ref, q_ref, k_ref, pos_ref, qw_ref, kw_ref,
             invf_ref, ma_ref, mb_ref, bd_ref, out_ref, dqw_ref, dkw_ref):
        ang = pos_ref[...].astype(F32) * invf_ref[...]
        cos = jnp.cos(ang)
        sin = jnp.sin(ang)
        s_a = sin * ma_ref[...]
        s_b = sin * mb_ref[...]
        bdv = bd_ref[...]
        first = pl.program_id(0) == 0
        for src, dsrc, w_ref, col0, dw_ref, sc in ((q_ref, dq_ref, qw_ref, 0, dqw_ref, scale),
                                                   (k_ref, dk_ref, kw_ref, width, dkw_ref, 1.0)):
            dw_acc = jnp.zeros((1, LANES), F32)
            for cb in range(width // LANES):
                cols = slice(cb * LANES, (cb + 1) * LANES)
                t = src[:, cols]
                dr = dsrc[:, cols]
                if sc != 1.0:
                    dr = dr * sc
                dy = dr * cos + pltpu.roll(dr * s_a, 8, axis=1) + pltpu.roll(dr * s_b, LANES - 8, axis=1)
                rstd = lax.rsqrt(_head_sums(t * t, bdv) * (1.0 / HEAD_DIM) + EPS)
                xhat = t * rstd
                g = dy * w_ref[...]
                dt = rstd * (g - xhat * (_head_sums(g * xhat, bdv) * (1.0 / HEAD_DIM)))
                out_ref[:, col0 + cb * LANES: col0 + (cb + 1) * LANES] = dt.astype(BF16)
                dw_acc = dw_acc + jnp.sum(dy * xhat, axis=0, keepdims=True)
            dw_acc = dw_acc + pltpu.roll(dw_acc, HEAD_DIM, axis=1)

            @pl.when(first)
            def _(dw_ref=dw_ref, dw_acc=dw_acc):
                dw_ref[...] = dw_acc

            @pl.when(jnp.logical_not(first))
            def _(dw_ref=dw_ref, dw_acc=dw_acc):
                dw_ref[...] += dw_acc
        out_ref[:, 2 * width: 3 * width] = dv_ref[...].astype(BF16)
        out_ref[:, 3 * width: 3 * width + ch] = da_ref[...]
        out_ref[:, 3 * width + ch: 3 * width + 2 * ch] = db_ref[...]
        out_ref[:, 3 * width + 2 * ch: out_w] = dgl_ref[...]

    vec = pl.BlockSpec((1, LANES), lambda i: (0, 0))
    blk = lambda c: pl.BlockSpec((tm, width), lambda i: (i, c))
    cblk = pl.BlockSpec((tm, ch), lambda i: (i, 0))
    return _pallas(
        body, name=name, grid=(s // tm,),
        in_specs=[blk(0), blk(0), blk(0), cblk, cblk, pl.BlockSpec((tm, gate_w), lambda i: (i, 0)),
                  blk(0), blk(1), pl.BlockSpec((tm, 1), lambda i: (i, 0)), vec, vec, vec, vec, vec,
                  pl.BlockSpec((LANES, LANES), lambda i: (0, 0))],
        out_specs=[pl.BlockSpec((tm, out_w), lambda i: (i, 0)), vec, vec],
        out_shape=[jax.ShapeDtypeStruct((s, out_w), BF16)] + [jax.ShapeDtypeStruct((1, LANES), F32)] * 2,
        operands=[dqn, dkn, dv, da, db, dgl, proj, proj, pos_col, qw2, kw2, invf, m_a, m_b, bd],
        rider=rider)


def _row_chunks(n_rows, fn, chunk=256):
    def step(i, c):
        fn(pl.ds(pl.multiple_of(i * chunk, chunk), chunk))
        return c
    lax.fori_loop(0, n_rows // chunk, step, 0)


def _to_residue_major(dst, src, s, d, dst_off=0, cast=None):
    seq = s // d
    for r in range(d):
        v = src[...] if d == 1 else src[pl.ds(r, seq, stride=d), :]
        dst[dst_off + r * seq: dst_off + (r + 1) * seq, :] = v if cast is None else v.astype(cast)


def _from_residue_major(dst, src, s, d, src_off=0):
    seq = s // d
    for r in range(d):
        v = src[src_off + r * seq: src_off + (r + 1) * seq, :]
        if d == 1:
            dst[...] = v
        else:
            dst[pl.ds(r, seq, stride=d), :] = v


def _band_bias():
    qi = lax.broadcasted_iota(jnp.int32, (QBLK, KWIN), 0)
    kj = lax.broadcasted_iota(jnp.int32, (QBLK, KWIN), 1)
    return jnp.where(jnp.abs(kj - HALF_SPAN - qi) <= HALF_SPAN, 0.0, NEG_INF).astype(F32)


def _range_bias(base, seq):
    kj = lax.broadcasted_iota(jnp.int32, (1, KWIN), 1)
    lo = (base & -seq) - base + HALF_SPAN
    return jnp.where((kj >= lo) & (kj < lo + seq), 0.0, NEG_INF).astype(F32)


ATTN_BLOCKS_PER_TRIP = 4


def _skewed_blocks(n_blk, produce, consume):
    per = ATTN_BLOCKS_PER_TRIP
    produce(0, 0)

    def trip(i, carry):
        for u in range(per):
            consume(per * i + u, u % 2)
            produce(per * i + u + 1, (u + 1) % 2)
        return carry

    n_trips = n_blk // per - 1
    lax.fori_loop(0, n_trips, trip, 0)
    for b in range(per * n_trips, n_blk):
        consume(b, b % 2)
        if b + 1 < n_blk:
            produce(b + 1, (b + 1) % 2)


def _block_base(b):
    return b * QBLK if isinstance(b, int) else pl.multiple_of(b * QBLK, QBLK)


def _attn_fwd(qn, kn, proj, name, rider=None):
    s = qn.shape[0]
    n_pairs = N_SLOT_HEADS * HEAD_DIM // LANES
    v_col0 = 2 * qn.shape[1] // LANES
    nt_dims = (((1,), (1,)), ((), ()))

    def body(q_ref, k_ref, v_ref, attn_ref, lse_ref, attn_b_ref, q_rm, k_rm, v_rm, acc_rm, m_rm, l_rm,
             acc_p, m_p, l_p, m_run, l_run, acc_run, band, s_buf, m_buf):
        g = pl.program_id(1)
        zpad = jnp.zeros((HALF_SPAN, LANES), BF16)
        k_rm[0:HALF_SPAN, :] = zpad
        k_rm[s + HALF_SPAN: s + 2 * HALF_SPAN, :] = zpad
        v_rm[0:HALF_SPAN, 0:LANES] = zpad
        v_rm[s + HALF_SPAN: s + 2 * HALF_SPAN, 0:LANES] = zpad

        def ones_rows(rows):
            v_rm[pl.ds(rows.start, rows.size), LANES:2 * LANES] = jnp.ones((rows.size, LANES), BF16)

        _row_chunks(s + 2 * HALF_SPAN, ones_rows, chunk=2 * HALF_SPAN)
        band[...] = _band_bias()
        lane = lax.broadcasted_iota(jnp.int32, (QBLK, LANES), 1)
        low = lane < HEAD_DIM
        n_blk = s // QBLK

        for gi, d in enumerate(DILATIONS):
            @pl.when(g == gi)
            def _(gi=gi, d=d):
                seq = s // d
                _to_residue_major(q_rm, q_ref, s, d, cast=BF16)
                _to_residue_major(k_rm, k_ref, s, d, dst_off=HALF_SPAN, cast=BF16)
                _to_residue_major(v_rm.at[:, 0:LANES], v_ref, s, d, dst_off=HALF_SPAN, cast=BF16)

                def scores(b, slot):
                    base = _block_base(b)
                    q = q_rm[pl.ds(base, QBLK), :]
                    zero = jnp.zeros_like(q)
                    q2 = jnp.concatenate([jnp.where(low, q, zero), jnp.where(low, zero, q)], axis=0)
                    sc = lax.dot_general(q2, k_rm[pl.ds(base, KWIN), :], nt_dims, preferred_element_type=F32)
                    bias = band[...] + _range_bias(base, seq)
                    for hh in range(2):
                        rows = slice(hh * QBLK, (hh + 1) * QBLK)
                        sh = sc[rows, :] + bias
                        s_buf[slot, rows, :] = sh
                        m_buf[slot, rows, :] = jnp.broadcast_to(jnp.max(sh, axis=-1, keepdims=True), (QBLK, LANES))

                def outputs(b, slot):
                    base = _block_base(b)
                    sv = s_buf[slot]
                    mb = m_buf[slot]
                    p = jnp.exp(jnp.concatenate([sv[:, 0:LANES] - mb, sv[:, LANES:2 * LANES] - mb], axis=1))
                    pv = jnp.dot(p.astype(BF16), v_rm[pl.ds(base, KWIN), :], preferred_element_type=F32)
                    rows = pl.ds(base, QBLK)
                    acc_rm[rows, :] = jnp.where(low, pv[0:QBLK, 0:LANES], pv[QBLK:2 * QBLK, 0:LANES])
                    l_rm[rows, :] = jnp.where(low, pv[0:QBLK, LANES:2 * LANES], pv[QBLK:2 * QBLK, LANES:2 * LANES])
                    m_rm[rows, :] = jnp.where(low, mb[0:QBLK, :], mb[QBLK:2 * QBLK, :])

                _skewed_blocks(n_blk, scores, outputs)
                if d == 1:
                    src = (acc_rm, m_rm, l_rm)
                else:
                    for dst_, src_ in ((acc_p, acc_rm), (m_p, m_rm), (l_p, l_rm)):
                        _from_residue_major(dst_, src_, s, d)
                    src = (acc_p, m_p, l_p)

                def combine(rows):
                    a_g, m_g, l_g = src[0][rows, :], src[1][rows, :], src[2][rows, :]
                    if gi == 0:
                        m_new, l_new, a_new = m_g, l_g, a_g
                    else:
                        m_old = m_run[rows, :]
                        m_new = jnp.maximum(m_old, m_g)
                        w_old = jnp.exp(m_old - m_new)
                        w_g = jnp.exp(m_g - m_new)
                        l_new = l_run[rows, :] * w_old + l_g * w_g
                        a_new = acc_run[rows, :] * w_old + a_g * w_g
                    if gi == len(DILATIONS) - 1:
                        out = a_new / l_new
                        attn_ref[rows, :] = out
                        attn_b_ref[rows, :] = out.astype(BF16)
                        lse_ref[rows, :] = m_new + jnp.log(l_new)
                    else:
                        m_run[rows, :] = m_new
                        l_run[rows, :] = l_new
                        acc_run[rows, :] = a_new

                _row_chunks(s, combine)

    qk_spec = pl.BlockSpec((s, LANES), lambda hp, g: (0, g * n_pairs + hp))
    v_spec = pl.BlockSpec((s, LANES), lambda hp, g: (0, v_col0 + g * n_pairs + hp))
    o_spec = pl.BlockSpec((s, LANES), lambda hp, g: (0, hp))
    f32buf = pltpu.VMEM((s, LANES), F32)
    return _pallas(
        body, name=name, grid=(n_pairs, len(DILATIONS)), in_specs=[qk_spec, qk_spec, v_spec],
        out_specs=[o_spec, o_spec, o_spec],
        out_shape=[jax.ShapeDtypeStruct((s, n_pairs * LANES), F32)] * 2
        + [jax.ShapeDtypeStruct((s, n_pairs * LANES), BF16)],
        operands=[qn, kn, proj],
        scratch_shapes=[pltpu.VMEM((s, LANES), BF16), pltpu.VMEM((s + 2 * HALF_SPAN, LANES), BF16),
                        pltpu.VMEM((s + 2 * HALF_SPAN, 2 * LANES), BF16)] + [f32buf] * 9
        + [pltpu.VMEM((QBLK, KWIN), F32), pltpu.VMEM((2, 2 * QBLK, KWIN), F32),
           pltpu.VMEM((2, 2 * QBLK, LANES), F32)],
        rider=rider)


def _attn_bwd(qn, kn, proj, dattn, attn, lse, bd, name, rider=None):
    s = qn.shape[0]
    n_pairs = N_SLOT_HEADS * HEAD_DIM // LANES
    v_col0 = 2 * qn.shape[1] // LANES
    nt_dims = (((1,), (1,)), ((), ()))
    tn_dims = (((0,), (0,)), ((), ()))
    spad = s + 2 * HALF_SPAN

    def body(q_ref, k_ref, v_ref, do_ref, o_ref, lse_ref, bd_ref, dq_ref, dk_ref, dv_ref,
             q_rm, k_rm, v_rm, do_rm, lse0_rm, lse1_rm, dd0_rm, dd1_rm, dq_rm, dk_rm, dv_rm,
             lse0_p, lse1_p, dd0_p, dd1_p, band, p_buf, ds_buf):
        g = pl.program_id(1)
        zpad = jnp.zeros((HALF_SPAN, LANES), BF16)
        for buf in (k_rm, v_rm):
            buf[0:HALF_SPAN, :] = zpad
            buf[s + HALF_SPAN: spad, :] = zpad
        zf = jnp.zeros((HALF_SPAN, LANES), F32)
        for buf in (dk_rm, dv_rm):
            buf[0:HALF_SPAN, :] = zf
            buf[s + HALF_SPAN: spad, :] = zf
        band[...] = _band_bias()

        def clear(rows):
            z = jnp.zeros((rows.size, LANES), F32)
            dk_rm[pl.ds(rows.start + HALF_SPAN, rows.size), :] = z
            dv_rm[pl.ds(rows.start + HALF_SPAN, rows.size), :] = z

        _row_chunks(s, clear)

        def prepare(rows):
            lo = lax.broadcasted_iota(jnp.int32, (rows.size, LANES), 1) < HEAD_DIM
            dsum = _head_sums(do_ref[rows, :] * o_ref[rows, :], bd_ref[...])
            dswap = pltpu.roll(dsum, HEAD_DIM, axis=1)
            dd0_p[rows, :] = jnp.where(lo, dsum, dswap)
            dd1_p[rows, :] = jnp.where(lo, dswap, dsum)
            lv = lse_ref[rows, :]
            lswap = pltpu.roll(lv, HEAD_DIM, axis=1)
            lse0_p[rows, :] = jnp.where(lo, lv, lswap)
            lse1_p[rows, :] = jnp.where(lo, lswap, lv)

        @pl.when(g == 0)
        def _():
            _row_chunks(s, prepare)
        lane = lax.broadcasted_iota(jnp.int32, (QBLK, LANES), 1)
        low = lane < HEAD_DIM
        n_blk = s // QBLK

        def stacked(ref, rows):
            val = ref[rows, :]
            zero = jnp.zeros_like(val)
            return jnp.concatenate([jnp.where(low, val, zero), jnp.where(low, zero, val)], axis=0)

        for gi, d in enumerate(DILATIONS):
            @pl.when(g == gi)
            def _(d=d):
                seq = s // d
                _to_residue_major(q_rm, q_ref, s, d, cast=BF16)
                _to_residue_major(k_rm, k_ref, s, d, dst_off=HALF_SPAN, cast=BF16)
                _to_residue_major(v_rm, v_ref, s, d, dst_off=HALF_SPAN, cast=BF16)
                _to_residue_major(do_rm, do_ref, s, d, cast=BF16)
                for dst_, src_ in ((lse0_rm, lse0_p), (lse1_rm, lse1_p), (dd0_rm, dd0_p), (dd1_rm, dd1_p)):
                    _to_residue_major(dst_, src_, s, d)

                def scores(b, slot):
                    base = _block_base(b)
                    rows = pl.ds(base, QBLK)
                    win = pl.ds(base, KWIN)
                    sc = lax.dot_general(stacked(q_rm, rows), k_rm[win, :], nt_dims, preferred_element_type=F32)
                    dp = lax.dot_general(stacked(do_rm, rows), v_rm[win, :], nt_dims, preferred_element_type=F32)
                    bias = band[...] + _range_bias(base, seq)
                    for hh, (lse_r, dd_r) in enumerate(((lse0_rm, dd0_rm), (lse1_rm, dd1_rm))):
                        r = slice(hh * QBLK, (hh + 1) * QBLK)
                        lse_h = lse_r[rows, :]
                        dd_h = dd_r[rows, :]
                        sh = sc[r, :] + bias
                        p = jnp.exp(jnp.concatenate([sh[:, 0:LANES] - lse_h, sh[:, LANES:KWIN] - lse_h], axis=1))
                        dph = dp[r, :]
                        ds = p * jnp.concatenate([dph[:, 0:LANES] - dd_h, dph[:, LANES:KWIN] - dd_h], axis=1)
                        p_buf[slot, r, :] = p.astype(BF16)
                        ds_buf[slot, r, :] = ds.astype(BF16)

                def grads(b, slot):
                    base = _block_base(b)
                    rows = pl.ds(base, QBLK)
                    win = pl.ds(base, KWIN)
                    p = p_buf[slot]
                    ds = ds_buf[slot]
                    dq2 = jnp.dot(ds, k_rm[win, :], preferred_element_type=F32)
                    dq_rm[rows, :] = jnp.where(low, dq2[0:QBLK, :], dq2[QBLK:2 * QBLK, :])
                    dk_rm[win, :] += lax.dot_general(ds, stacked(q_rm, rows), tn_dims, preferred_element_type=F32)
                    dv_rm[win, :] += lax.dot_general(p, stacked(do_rm, rows), tn_dims, preferred_element_type=F32)

                _skewed_blocks(n_blk, scores, grads)
                _from_residue_major(dq_ref, dq_rm, s, d)
                _from_residue_major(dk_ref, dk_rm, s, d, src_off=HALF_SPAN)
                _from_residue_major(dv_ref, dv_rm, s, d, src_off=HALF_SPAN)

    qk_spec = pl.BlockSpec((s, LANES), lambda hp, g: (0, g * n_pairs + hp))
    v_spec = pl.BlockSpec((s, LANES), lambda hp, g: (0, v_col0 + g * n_pairs + hp))
    o_spec = pl.BlockSpec((s, LANES), lambda hp, g: (0, hp))
    width = qn.shape[1]
    f32buf = pltpu.VMEM((s, LANES), F32)
    f32pad = pltpu.VMEM((spad, LANES), F32)
    return _pallas(
        body, name=name, grid=(n_pairs, len(DILATIONS)),
        in_specs=[qk_spec, qk_spec, v_spec, o_spec, o_spec, o_spec,
                  pl.BlockSpec((LANES, LANES), lambda hp, g: (0, 0))],
        out_specs=[qk_spec, qk_spec, qk_spec],
        out_shape=[jax.ShapeDtypeStruct((s, width), F32)] * 3,
        operands=[qn, kn, proj, dattn, attn, lse, bd],
        scratch_shapes=[pltpu.VMEM((s, LANES), BF16), pltpu.VMEM((spad, LANES), BF16),
                        pltpu.VMEM((spad, LANES), BF16), pltpu.VMEM((s, LANES), BF16),
                        f32buf, f32buf, f32buf, f32buf, f32buf, f32pad, f32pad,
                        f32buf, f32buf, f32buf, f32buf, pltpu.VMEM((QBLK, KWIN), F32),
                        pltpu.VMEM((2, 2 * QBLK, KWIN), BF16), pltpu.VMEM((2, 2 * QBLK, KWIN), BF16)],
        rider=rider)


CONV_PAD = 16


def _conv_fwd(proj, conv_w, conv_b, col0, name, rider=None):
    s = proj.shape[0]
    ch = conv_w.shape[1]
    nblk = ch // LANES
    a0 = col0 // LANES
    tr = 256
    shift = CONV_PAD - (CONV_WIDTH - 1) // 2

    def body(a_ref, b_ref, w_ref, bias_ref, u0_ref, uc_ref, pad):
        z = jnp.zeros((CONV_PAD, LANES), F32)
        pad[0:CONV_PAD, :] = z
        pad[s + CONV_PAD: s + 2 * CONV_PAD, :] = z

        def glu(rows):
            u0 = a_ref[rows, :] * jax.nn.sigmoid(b_ref[rows, :])
            u0_ref[rows, :] = u0
            pad[pl.ds(rows.start + CONV_PAD, rows.size), :] = u0

        _row_chunks(s, glu)
        for t in range(0, s, tr):
            acc = jnp.broadcast_to(bias_ref[...], (tr, LANES))
            for k in range(CONV_WIDTH):
                acc = acc + w_ref[k:k + 1, :] * pad[t + k + shift: t + k + shift + tr, :]
            uc_ref[t:t + tr, :] = acc

    return _pallas(
        body, name=name, grid=(nblk,),
        in_specs=[pl.BlockSpec((s, LANES), lambda c: (0, a0 + c)),
                  pl.BlockSpec((s, LANES), lambda c: (0, a0 + nblk + c)),
                  pl.BlockSpec((CONV_WIDTH, LANES), lambda c: (0, c)),
                  pl.BlockSpec((1, LANES), lambda c: (0, c))],
        out_specs=[pl.BlockSpec((s, LANES), lambda c: (0, c))] * 2,
        out_shape=[jax.ShapeDtypeStruct((s, ch), F32)] * 2, operands=[proj, proj, conv_w, conv_b],
        scratch_shapes=[pltpu.VMEM((s + 2 * CONV_PAD, LANES), F32)], rider=rider)


def _ln_silu_fwd(uc, ln_w, ln_b, name):
    s, ch = uc.shape
    tm = 256

    def body(u_ref, w_ref, b_ref, o_ref):
        u = u_ref[...]
        mu = jnp.mean(u, axis=-1, keepdims=True)
        xc = u - mu
        rstd = lax.rsqrt(jnp.mean(xc * xc, axis=-1, keepdims=True) + EPS)
        z = xc * rstd * w_ref[...] + b_ref[...]
        o_ref[...] = (z * jax.nn.sigmoid(z)).astype(BF16)

    row = pl.BlockSpec((tm, ch), lambda i: (i, 0))
    vec = pl.BlockSpec((1, ch), lambda i: (0, 0))
    return _pinned_call(
        body, name=name, grid=(s // tm,), in_specs=[row, vec, vec], out_specs=row,
        out_shape=jax.ShapeDtypeStruct((s, ch), BF16), compiler_params=_params(),
    )(uc, ln_w, ln_b)


def _ln_silu_bwd(du3, uc, ln_w, ln_b, name):
    s, ch = uc.shape
    tm = 256

    def body(d_ref, u_ref, w_ref, b_ref, du_ref, dw_ref, db_ref):
        u = u_ref[...]
        mu = jnp.mean(u, axis=-1, keepdims=True)
        xc = u - mu
        rstd = lax.rsqrt(jnp.mean(xc * xc, axis=-1, keepdims=True) + EPS)
        xhat = xc * rstd
        z = xhat * w_ref[...] + b_ref[...]
        sg = jax.nn.sigmoid(z)
        dz = d_ref[...] * (sg * (1.0 + z * (1.0 - sg)))
        dxh = dz * w_ref[...]
        du_ref[...] = rstd * (dxh - jnp.mean(dxh, axis=-1, keepdims=True)
                              - xhat * jnp.mean(dxh * xhat, axis=-1, keepdims=True))
        pw = jnp.sum(dz * xhat, axis=0, keepdims=True)
        pb = jnp.sum(dz, axis=0, keepdims=True)
        first = pl.program_id(0) == 0

        @pl.when(first)
        def _():
            dw_ref[...] = pw
            db_ref[...] = pb

        @pl.when(jnp.logical_not(first))
        def _():
            dw_ref[...] += pw
            db_ref[...] += pb

    row = pl.BlockSpec((tm, ch), lambda i: (i, 0))
    vec = pl.BlockSpec((1, ch), lambda i: (0, 0))
    return _pinned_call(
        body, name=name, grid=(s // tm,), in_specs=[row, row, vec, vec], out_specs=[row, vec, vec],
        out_shape=[jax.ShapeDtypeStruct((s, ch), F32), jax.ShapeDtypeStruct((1, ch), F32),
                   jax.ShapeDtypeStruct((1, ch), F32)],
        compiler_params=_params(),
    )(du3, uc, ln_w, ln_b)


def _conv_bwd(duc, u0, proj, conv_w, col0, name, rider=None):
    s = proj.shape[0]
    ch = conv_w.shape[1]
    nblk = ch // LANES
    a0 = col0 // LANES
    tr = 256
    half = (CONV_WIDTH - 1) // 2
    shift = CONV_PAD - half

    def body(duc_ref, u0_ref, a_ref, b_ref, w_ref, da_ref, db_ref, dw_ref, dbias_ref, pad_d, pad_u):
        z = jnp.zeros((CONV_PAD, LANES), F32)
        for buf in (pad_d, pad_u):
            buf[0:CONV_PAD, :] = z
            buf[s + CONV_PAD: s + 2 * CONV_PAD, :] = z

        def fill(rows):
            dst = pl.ds(rows.start + CONV_PAD, rows.size)
            pad_d[dst, :] = duc_ref[rows, :]
            pad_u[dst, :] = u0_ref[rows, :]

        _row_chunks(s, fill)
        dw_acc = [jnp.zeros((8, LANES), F32) for _ in range(CONV_WIDTH)]
        dbias_acc = jnp.zeros((8, LANES), F32)
        for t in range(0, s, tr):
            d_t = duc_ref[t:t + tr, :]
            dbias_acc = dbias_acc + jnp.sum(d_t.reshape(tr // 8, 8, LANES), axis=0)
            du0 = jnp.zeros((tr, LANES), F32)
            for k in range(CONV_WIDTH):
                du0 = du0 + w_ref[k:k + 1, :] * pad_d[t - k + half + CONV_PAD: t - k + half + CONV_PAD + tr, :]
                prod = d_t * pad_u[t + k + shift: t + k + shift + tr, :]
                dw_acc[k] = dw_acc[k] + jnp.sum(prod.reshape(tr // 8, 8, LANES), axis=0)
            av = a_ref[t:t + tr, :]
            sg = jax.nn.sigmoid(b_ref[t:t + tr, :])
            da_ref[t:t + tr, :] = (du0 * sg).astype(BF16)
            db_ref[t:t + tr, :] = (du0 * av * sg * (1.0 - sg)).astype(BF16)
        for k in range(CONV_WIDTH):
            dw_ref[k:k + 1, :] = jnp.sum(dw_acc[k], axis=0, keepdims=True)
        dbias_ref[...] = jnp.sum(dbias_acc, axis=0, keepdims=True)

    col = lambda off: pl.BlockSpec((s, LANES), lambda c: (0, off + c))
    return _pallas(
        body, name=name, grid=(nblk,),
        in_specs=[col(0), col(0), col(a0), col(a0 + nblk),
                  pl.BlockSpec((CONV_WIDTH, LANES), lambda c: (0, c))],
        out_specs=[col(0), col(0), pl.BlockSpec((CONV_WIDTH, LANES), lambda c: (0, c)),
                   pl.BlockSpec((1, LANES), lambda c: (0, c))],
        out_shape=[jax.ShapeDtypeStruct((s, ch), BF16)] * 2
        + [jax.ShapeDtypeStruct((CONV_WIDTH, ch), F32), jax.ShapeDtypeStruct((1, ch), F32)],
        operands=[duc, u0, proj, proj, conv_w],
        scratch_shapes=[pltpu.VMEM((s + 2 * CONV_PAD, LANES), F32)] * 2, rider=rider)


GATE_BLK = 512


def _gate_fwd(proj, bg, y_a, y_b, col0, name):
    s, d = y_a.shape
    tm = 256
    g0 = col0 // GATE_BLK
    nb = d // GATE_BLK

    def body(ga_ref, gb_ref, ba_ref, bb_ref, ya_ref, yb_ref, o_ref):
        g_a = jax.nn.sigmoid(ga_ref[...] + ba_ref[...])
        g_b = jax.nn.sigmoid(gb_ref[...] + bb_ref[...])
        o_ref[...] = (g_a * ya_ref[...] + g_b * yb_ref[...]).astype(BF16)

    act = pl.BlockSpec((tm, GATE_BLK), lambda i, j: (i, j))
    return _pinned_call(
        body, name=name, grid=(s // tm, nb),
        in_specs=[pl.BlockSpec((tm, GATE_BLK), lambda i, j: (i, g0 + j)),
                  pl.BlockSpec((tm, GATE_BLK), lambda i, j: (i, g0 + nb + j)),
                  pl.BlockSpec((None, 1, GATE_BLK), lambda i, j: (0, 0, j)),
                  pl.BlockSpec((None, 1, GATE_BLK), lambda i, j: (1, 0, j)), act, act],
        out_specs=act, out_shape=jax.ShapeDtypeStruct((s, d), BF16), compiler_params=_params(),
    )(proj, proj, bg, bg, y_a, y_b)


def _gate_bwd(d_mixed, proj, bg, y_a, y_b, col0, name, rider=None):
    s, d = y_a.shape
    tm = 256
    half = d // 2
    assert col0 % half == 0
    c0 = col0 // half

    def body(dm_ref, a0_ref, a1_ref, b0_ref, b1_ref, bias_ref, ya_ref, yb_ref, dgl_ref, dya_ref, dyb_ref, db_ref):
        dm = dm_ref[...]
        parts = []
        for br, (lo_ref, hi_ref, y_ref, dy_ref) in enumerate(((a0_ref, a1_ref, ya_ref, dya_ref),
                                                              (b0_ref, b1_ref, yb_ref, dyb_ref))):
            logits = jnp.concatenate([lo_ref[...], hi_ref[...]], axis=1)
            gate = jax.nn.sigmoid(logits + bias_ref[br])
            dy_ref[...] = (dm * gate).astype(BF16)
            dgl = dm * y_ref[...] * gate * (1.0 - gate)
            dgl_ref[:, br * d:(br + 1) * d] = dgl.astype(BF16)
            parts.append(jnp.sum(dgl, axis=0, keepdims=True))
        part = jnp.concatenate(parts, axis=0)
        first = pl.program_id(0) == 0

        @pl.when(first)
        def _():
            db_ref[...] = part

        @pl.when(jnp.logical_not(first))
        def _():
            db_ref[...] += part

    row = pl.BlockSpec((tm, d), lambda i: (i, 0))
    logit_blk = lambda k: pl.BlockSpec((tm, half), functools.partial(lambda i, k: (i, c0 + k), k=k))
    return _pallas(
        body, name=name, grid=(s // tm,),
        in_specs=[row, logit_blk(0), logit_blk(1), logit_blk(2), logit_blk(3),
                  pl.BlockSpec((2, 1, d), lambda i: (0, 0, 0)), row, row],
        out_specs=[pl.BlockSpec((tm, 2 * d), lambda i: (i, 0)), row, row, pl.BlockSpec((2, d), lambda i: (0, 0))],
        out_shape=[jax.ShapeDtypeStruct((s, 2 * d), BF16), jax.ShapeDtypeStruct((s, d), BF16),
                   jax.ShapeDtypeStruct((s, d), BF16), jax.ShapeDtypeStruct((2, d), F32)],
        operands=[d_mixed, proj, proj, proj, proj, bg, y_a, y_b], rider=rider)


def _ffn_in_swiglu(h2, w_blocked, name):
    s, k = h2.shape
    nblk, _, tn = w_blocked.shape
    ff = nblk // 2 * tn
    tm = 512

    def body(a_ref, wg_ref, wu_ref, g_ref, u_ref, act_ref):
        a = a_ref[...]
        gt = jnp.dot(a, wg_ref[...], preferred_element_type=F32)
        up = jnp.dot(a, wu_ref[...], preferred_element_type=F32)
        g_ref[...] = gt
        u_ref[...] = up
        act_ref[...] = (gt * jax.nn.sigmoid(gt) * up).astype(BF16)

    out = pl.BlockSpec((tm, tn), lambda i, j: (i, j))
    return _pinned_call(
        body, name=name, grid=(s // tm, nblk // 2),
        in_specs=[pl.BlockSpec((tm, k), lambda i, j: (i, 0)),
                  pl.BlockSpec((None, k, tn), lambda i, j: (j, 0, 0)),
                  pl.BlockSpec((None, k, tn), lambda i, j: (nblk // 2 + j, 0, 0))],
        out_specs=[out, out, out],
        out_shape=[jax.ShapeDtypeStruct((s, ff), F32), jax.ShapeDtypeStruct((s, ff), F32),
                   jax.ShapeDtypeStruct((s, ff), BF16)],
        compiler_params=_params(),
    )(h2, w_blocked, w_blocked)


def _swiglu_bwd(gate, up, d_act, name, rider=None):
    s, ff = gate.shape
    tm = 256

    def body(g_ref, u_ref, d_ref, o_ref):
        gt = g_ref[...]
        sg = jax.nn.sigmoid(gt)
        dv = d_ref[...]
        o_ref[:, 0:ff] = (dv * u_ref[...] * (sg * (1.0 + gt * (1.0 - sg)))).astype(BF16)
        o_ref[:, ff:2 * ff] = (dv * gt * sg).astype(BF16)

    row = pl.BlockSpec((tm, ff), lambda i: (i, 0))
    return _pallas(
        body, name=name, grid=(s // tm,), in_specs=[row, row, row],
        out_specs=pl.BlockSpec((tm, 2 * ff), lambda i: (i, 0)),
        out_shape=jax.ShapeDtypeStruct((s, 2 * ff), BF16), operands=[gate, up, d_act], rider=rider)


def _out_proj_rmsnorm(mixed, w_out, x, norm_w, name):
    s, k = mixed.shape
    d = w_out.shape[1]
    tm = 512

    def body(a_ref, w_ref, x_ref, nw_ref, x1_ref, h2_ref):
        x1 = x_ref[...] + jnp.dot(a_ref[...], w_ref[...], preferred_element_type=F32)
        x1_ref[...] = x1
        rstd = lax.rsqrt(jnp.mean(x1 * x1, axis=-1, keepdims=True) + EPS)
        h2_ref[...] = (x1 * rstd * nw_ref[...]).astype(BF16)

    row = pl.BlockSpec((tm, d), lambda i: (i, 0))
    return _pinned_call(
        body, name=name, grid=(s // tm,),
        in_specs=[pl.BlockSpec((tm, k), lambda i: (i, 0)), pl.BlockSpec((k, d), lambda i: (0, 0)), row,
                  pl.BlockSpec((1, d), lambda i: (0, 0))],
        out_specs=[row, row],
        out_shape=[jax.ShapeDtypeStruct((s, d), F32), jax.ShapeDtypeStruct((s, d), BF16)],
        compiler_params=_params(),
    )(mixed, w_out, x, norm_w)


def _ffn_out_loss(act, w_ffn_out, x1, target, name):
    s, k = act.shape
    d = w_ffn_out.shape[1]
    tm = 512

    def body(a_ref, w_ref, x1_ref, t_ref, dy_ref, dyb_ref, loss_ref, acc):
        y = x1_ref[...] + jnp.dot(a_ref[...], w_ref[...], preferred_element_type=F32)
        diff = y - t_ref[...]
        dy = diff * (1.0 / d)
        dy_ref[...] = dy
        dyb_ref[...] = dy.astype(BF16)
        part = jnp.sum((diff * diff).reshape(tm // 8, 8, d), axis=0)
        i = pl.program_id(0)

        @pl.when(i == 0)
        def _():
            acc[...] = part

        @pl.when(i > 0)
        def _():
            acc[...] += part

        @pl.when(i == pl.num_programs(0) - 1)
        def _():
            loss_ref[...] = (0.5 / d) * jnp.sum(jnp.sum(acc[...], axis=1, keepdims=True), axis=0, keepdims=True)

    row = pl.BlockSpec((tm, d), lambda i: (i, 0))
    return _pinned_call(
        body, name=name, grid=(s // tm,),
        in_specs=[pl.BlockSpec((tm, k), lambda i: (i, 0)), pl.BlockSpec((k, d), lambda i: (0, 0)), row, row],
        out_specs=[row, row, pl.BlockSpec((1, 1), lambda i: (0, 0))],
        out_shape=[jax.ShapeDtypeStruct((s, d), F32), jax.ShapeDtypeStruct((s, d), BF16),
                   jax.ShapeDtypeStruct((1, 1), F32)],
        scratch_shapes=[pltpu.VMEM((8, d), F32)], compiler_params=_params(),
    )(act, w_ffn_out, x1, target)


LATE_GATHER = ("w_o_attn", "w_pw_conv", "w_out", "w_ffn_in", "w_ffn_out")
EARLY_REDUCE = LATE_GATHER


def _blocks_by_half(g):
    if g.ndim == 2:
        g = g.reshape(N_CHIPS, g.shape[0] // N_CHIPS, g.shape[1])
    return g.reshape(N_CHIPS, 2, g.shape[1] // 2, g.shape[2])


def _forward_backward(x, pos_col, target, wts, late_bufs, pos_arr):
    wts = dict(wts)
    consts = _rope_consts()
    bd = consts[3]
    qw2 = jnp.tile(wts["q_norm_w"], (1, LANES // HEAD_DIM))
    kw2 = jnp.tile(wts["k_norm_w"], (1, LANES // HEAD_DIM))
    qkv_w = 3 * N_SLOT_HEADS * HEAD_DIM
    conv_col0 = 3 * qkv_w
    ch = wts["conv_w"].shape[1]
    gate_col0 = conv_col0 + 2 * ch

    h = _rmsnorm_fwd(x, wts["norm1_w"], "rms1_fwd")
    late = dict(zip(LATE_GATHER, late_bufs))
    quarter = late["w_ffn_in"].shape[2] // 4
    proj, (late["w_o_attn"], late["w_pw_conv"], late["w_out"], late["w_ffn_in"]) = _matmul(
        h, wts["w_in"], mode="nn", tm=512, tn=1920, tk=1024, out_dtype=F32, name="mm_proj", b_blocked=True,
        rider=_gather_ici_rider([late["w_o_attn"], late["w_pw_conv"], late["w_out"], late["w_ffn_in"]], [],
                                row_ranges=[None, None, None, (0, quarter)]))
    (qn, kn), (late["w_ffn_in"],) = _qk_fwd(
        proj, pos_col, qw2, kw2, consts, "qk_fwd",
        rider=_gather_ici_rider([late["w_ffn_in"]], [], row_ranges=[(quarter, 2 * quarter)]))
    (attn, lse, attn_b), (late["w_ffn_in"], late["w_ffn_out"]) = _attn_fwd(
        qn, kn, proj, "attn_fwd",
        rider=_gather_ici_rider([late["w_ffn_in"], late["w_ffn_out"]], [],
                                row_ranges=[(3 * quarter, quarter), None]))
    (u0, uc), late_bufs = _conv_fwd(proj, wts["conv_w"], wts["conv_b"], conv_col0, "conv_fwd",
                                    rider=_gather_forward_rider([late[n] for n in LATE_GATHER]))
    for n, buf in zip(LATE_GATHER, late_bufs):
        full = buf.reshape(N_CHIPS, -1, buf.shape[3])
        wts[n] = full.reshape(-1, full.shape[2]) if n in ROW_SHARDED else full
    y_a = _matmul(attn_b, wts["w_o_attn"], mode="nn", tm=1024, tn=256, tk=512, out_dtype=F32, name="mm_ya",
                  b_blocked=True)
    u3 = _ln_silu_fwd(uc, wts["conv_ln_w"], wts["conv_ln_b"], "ln_fwd")
    y_b = _matmul(u3, wts["w_pw_conv"], mode="nn", tm=1024, tn=256, tk=512, out_dtype=F32, name="mm_yb",
                  b_blocked=True)
    mixed = _gate_fwd(proj, wts["b_gate"], y_a, y_b, gate_col0, "gate_fwd")
    x1, h2 = _out_proj_rmsnorm(mixed, wts["w_out"], x, wts["norm2_w"], "mm_x1_rms2")
    gate, up, act = _ffn_in_swiglu(h2, wts["w_ffn_in"], "mm_gu_swiglu")
    dy, dy_b16, loss = _ffn_out_loss(act, wts["w_ffn_out"], x1, target, "mm_x2_loss")

    g = {}
    by_chip = {}

    def pair_add(n, blocks, received):
        return _add_own_half(blocks, received, pos_arr, f"grads_pair_add_{n}")

    d_act = _matmul(dy_b16, wts["w_ffn_out"], mode="nt", tm=512, tn=1408, tk=1024, out_dtype=F32, name="mm_dact")
    g_ffn_out = _blocks_by_half(
        _matmul(act, dy_b16, mode="tn", tm=1408, tn=1024, tk=2048, out_dtype=F32, name="mm_dwffnout"))
    dgu, (received,) = _swiglu_bwd(gate, up, d_act, "swiglu_bwd",
                                   rider=_pair_exchange_rider([g_ffn_out], halved=True))
    to_send, own = pair_add("w_ffn_out", g_ffn_out, received)
    dh2, (by_chip["w_ffn_out"],) = _matmul(
        dgu, wts["w_ffn_in"], mode="nt", tm=1024, tn=1024, tk=1408, out_dtype=F32, name="mm_dh2", b_blocked=True,
        rider=_chip_exchange_rider([to_send], [own]))
    g_ffn_in = _blocks_by_half(_matmul(h2, dgu, mode="tn", tm=512, tn=1408, tk=2048, out_dtype=F32,
                                       name="mm_dwffnin", out_blocked=N_CHIPS))
    dx1, dx1_b16, g["norm2_w"] = _rmsnorm_bwd(dh2, x1, wts["norm2_w"], dy, "rms2_bwd")
    d_mixed = _matmul(dx1_b16, wts["w_out"], mode="nt", tm=512, tn=1024, tk=1024, out_dtype=F32, name="mm_dmixed")
    g["w_out"] = _matmul(mixed, dx1_b16, mode="tn", tm=512, tn=1024, tk=2048, out_dtype=F32, name="mm_dwout")
    (dgl, dy_a, dy_b, g["b_gate"]), (received,) = _gate_bwd(
        d_mixed, proj, wts["b_gate"], y_a, y_b, gate_col0, "gate_bwd",
        rider=_pair_exchange_rider([g_ffn_in], halved=True))
    ffn_in_to_send, ffn_in_own = pair_add("w_ffn_in", g_ffn_in, received)
    dattn = _matmul(dy_a, wts["w_o_attn"], mode="nt", tm=1024, tn=512, tk=256, out_dtype=F32, name="mm_dattn",
                    b_blocked=True)
    g["w_o_attn"] = _matmul(attn_b, dy_a, mode="tn", tm=512, tn=256, tk=2048, out_dtype=F32, name="mm_dwo",
                            out_blocked=N_CHIPS)
    du3 = _matmul(dy_b, wts["w_pw_conv"], mode="nt", tm=1024, tn=512, tk=256, out_dtype=F32, name="mm_du3",
                  b_blocked=True)
    g["w_pw_conv"] = _matmul(u3, dy_b, mode="tn", tm=512, tn=256, tk=2048, out_dtype=F32, name="mm_dwpw",
                             out_blocked=N_CHIPS)
    duc, g["conv_ln_w"], g["conv_ln_b"] = _ln_silu_bwd(du3, uc, wts["conv_ln_w"], wts["conv_ln_b"], "ln_bwd")

    small3 = ("w_out", "w_o_attn", "w_pw_conv")
    g_small3 = [_blocks_by_half(g.pop(n)) for n in small3]
    (da, db, g["conv_w"], g["conv_b"]), received = _conv_bwd(
        duc, u0, proj, wts["conv_w"], conv_col0, "conv_bwd", rider=_pair_exchange_rider(g_small3, halved=True))
    sums3 = [pair_add(n, gb, rv) for n, gb, rv in zip(small3, g_small3, received)]
    (dqn, dkn, dv), (by_chip["w_ffn_in"],) = _attn_bwd(
        qn, kn, proj, dattn, attn, lse, bd, "attn_bwd",
        rider=_chip_exchange_rider([ffn_in_to_send], [ffn_in_own]))
    (dproj, dqw, dkw), exchanged3 = _qk_bwd(
        dqn, dkn, dv, da, db, dgl, proj, pos_col, qw2, kw2, consts, "qk_bwd",
        rider=_chip_exchange_rider([s[0] for s in sums3], [s[1] for s in sums3]))
    by_chip.update(zip(small3, exchanged3))
    halves = [_sum_chips(by_chip[n], pos_arr, f"grads_chip_sum_{n}") for n in EARLY_REDUCE]
    g["q_norm_w"] = dqw[:, :HEAD_DIM]
    g["k_norm_w"] = dkw[:, :HEAD_DIM]

    c = pos_arr[0]
    rh = h.shape[1] // 2
    h_sibling = lax.dynamic_slice_in_dim(h, (1 - c) * rh, rh, axis=1)
    h_own = lax.dynamic_slice_in_dim(h, c * rh, rh, axis=1)
    g_sibling, shards = _matmul(h_sibling, dproj, mode="tn", tm=rh, tn=1920, tk=2048, out_dtype=F32,
                                name="mm_dwin_sibling", out_blocked=N_CHIPS, rider=_pair_gather_rider(halves))
    reduced = dict(zip(EARLY_REDUCE, shards))
    g_own, from_sibling = _matmul(h_own, dproj, mode="tn", tm=rh, tn=1920, tk=2048, out_dtype=F32,
                                  name="mm_dwin_own", out_blocked=N_CHIPS,
                                  rider=_pair_exchange_rider([g_sibling], halved=False))
    to_send, own = _add_own_half(g_own, from_sibling[0], pos_arr, "grads_pair_add_w_in")
    *in_flight, token = _chip_exchange_start(to_send, own, "grads_w_in_exchange_start")
    in_flight = tuple(in_flight) + (token,)
    dh = _matmul(dproj, wts["w_in"], mode="nt", tm=1024, tn=1024, tk=1920, out_dtype=F32, name="mm_dh",
                 b_blocked=True, after=[token])
    grad_x, _, g["norm1_w"] = _rmsnorm_bwd(dh, x, wts["norm1_w"], dx1, "rms1_bwd")
    return loss, grad_x, g, reduced, in_flight


def _mesh_pos():
    return lax.axis_index("x"), lax.axis_index("y"), lax.axis_index("c")


def _other_chips(x, y):
    return [(1 - x, y), (x, 1 - y), (1 - x, 1 - y)]


def _cast_into_slot(shard, chip_arr, dtype, name):
    r, c = shard.shape
    tr = r // 2 if r % 32 == 0 else r

    def body(chip_ref, s_ref, o_ref):
        del chip_ref
        o_ref[...] = s_ref[...].astype(dtype)

    return _pinned_call(
        body, name=name,
        grid_spec=pltpu.PrefetchScalarGridSpec(
            num_scalar_prefetch=1, grid=(r // tr,),
            in_specs=[pl.BlockSpec((tr, c), lambda i, chip_ref: (i, 0))],
            out_specs=pl.BlockSpec((None, tr, c), lambda i, chip_ref: (chip_ref[0], i, 0))),
        out_shape=jax.ShapeDtypeStruct((N_CHIPS, r, c), dtype), compiler_params=_params(),
    )(chip_arr, shard)


def _allgather_inplace(big, small, name):
    nb, ns = len(big), len(small)
    n = nb + ns

    def body(*refs):
        bufs = refs[n:2 * n]
        send_sems, recv_sems, fsend_sems, frecv_sems = refs[2 * n:]
        x, y, c = _mesh_pos()
        me = 2 * x + y
        chips = _other_chips(x, y)

        def part(a, slot, half):
            return bufs[a].at[slot, half] if a < nb else bufs[a].at[slot]

        sends = []
        for a in range(n):
            for k, (px, py) in enumerate(chips):
                cp = pltpu.make_async_remote_copy(
                    src_ref=part(a, me, c), dst_ref=part(a, me, c), send_sem=send_sems.at[a, k],
                    recv_sem=recv_sems.at[a, k], device_id=(px, py, c), device_id_type=MESH)
                cp.start()
                sends.append(cp)
        for a in range(n):
            for k, (px, py) in enumerate(chips):
                slot = 2 * px + py
                pltpu.make_async_remote_copy(
                    src_ref=part(a, slot, c), dst_ref=part(a, slot, c), send_sem=send_sems.at[a, k],
                    recv_sem=recv_sems.at[a, k], device_id=(px, py, c), device_id_type=MESH).wait_recv()
                if a < nb:
                    fwd = pltpu.make_async_remote_copy(
                        src_ref=part(a, slot, c), dst_ref=part(a, slot, c), send_sem=fsend_sems.at[a, k],
                        recv_sem=frecv_sems.at[a, k], device_id=(x, y, 1 - c), device_id_type=MESH)
                    fwd.start()
                    sends.append(fwd)
        for a in range(nb):
            for k, (px, py) in enumerate(chips):
                slot = 2 * px + py
                pltpu.make_async_remote_copy(
                    src_ref=part(a, slot, 1 - c), dst_ref=part(a, slot, 1 - c), send_sem=fsend_sems.at[a, k],
                    recv_sem=frecv_sems.at[a, k], device_id=(x, y, 1 - c), device_id_type=MESH).wait_recv()
        for cp in sends:
            cp.wait_send()

    ops = list(big) + list(small)
    return _pinned_call(
        body, name=name, in_specs=[ANY] * n, out_specs=[ANY] * n,
        out_shape=[jax.ShapeDtypeStruct(o.shape, o.dtype) for o in ops],
        input_output_aliases={i: i for i in range(n)},
        scratch_shapes=[pltpu.SemaphoreType.DMA((n, 3)), pltpu.SemaphoreType.DMA((n, 3)),
                        pltpu.SemaphoreType.DMA((nb, 3)), pltpu.SemaphoreType.DMA((nb, 3))],
    )(*ops)


def _comm_call(rider, name):
    def body():
        pass

    return _pallas(body, name=name, grid=(1,), in_specs=[], out_specs=[], out_shape=[], operands=[],
                   rider=rider)[1]


def _gather_ici_rider(big, small, row_ranges=None):
    nb = len(big)
    n = nb + len(small)
    row_ranges = row_ranges or [None] * nb

    def copies(bufs, sems):
        x, y, c = _mesh_pos()
        me = 2 * x + y

        def part(a, slot):
            if a >= nb:
                return bufs[a].at[slot]
            if row_ranges[a] is None:
                return bufs[a].at[slot, c]
            return bufs[a].at[slot, c, pl.ds(*row_ranges[a])]
        out = []
        for a in range(n):
            for k, (px, py) in enumerate(_other_chips(x, y)):
                send = functools.partial(
                    pltpu.make_async_remote_copy,
                    src_ref=part(a, me), dst_ref=part(a, me), send_sem=sems[0].at[a, k],
                    recv_sem=sems[1].at[a, k], device_id=(px, py, c), device_id_type=MESH)
                recv = functools.partial(
                    pltpu.make_async_remote_copy,
                    src_ref=part(a, 2 * px + py), dst_ref=part(a, 2 * px + py), send_sem=sems[0].at[a, k],
                    recv_sem=sems[1].at[a, k], device_id=(px, py, c), device_id_type=MESH)
                out.append((send, recv))
        return out

    def start(r_in, r_out, sems):
        for send, _ in copies(r_out, sems):
            send().start()

    def wait(r_in, r_out, sems):
        cps = copies(r_out, sems)
        for _, recv in cps:
            recv().wait_recv()
        for send, _ in cps:
            send().wait_send()

    ops = list(big) + list(small)
    return _Rider(ops, [jax.ShapeDtypeStruct(o.shape, o.dtype) for o in ops], {i: i for i in range(n)},
                  [pltpu.SemaphoreType.DMA((n, 3)), pltpu.SemaphoreType.DMA((n, 3))], start, wait)


def _gather_forward_rider(big):
    n = len(big)

    def copies(bufs, sems):
        x, y, c = _mesh_pos()
        out = []
        for a in range(n):
            for k, (px, py) in enumerate(_other_chips(x, y)):
                slot = 2 * px + py
                send = functools.partial(
                    pltpu.make_async_remote_copy,
                    src_ref=bufs[a].at[slot, c], dst_ref=bufs[a].at[slot, c], send_sem=sems[0].at[a, k],
                    recv_sem=sems[1].at[a, k], device_id=(x, y, 1 - c), device_id_type=MESH)
                recv = functools.partial(
                    pltpu.make_async_remote_copy,
                    src_ref=bufs[a].at[slot, 1 - c], dst_ref=bufs[a].at[slot, 1 - c], send_sem=sems[0].at[a, k],
                    recv_sem=sems[1].at[a, k], device_id=(x, y, 1 - c), device_id_type=MESH)
                out.append((send, recv))
        return out

    def start(r_in, r_out, sems):
        for send, _ in copies(r_out, sems):
            send().start()

    def wait(r_in, r_out, sems):
        cps = copies(r_out, sems)
        for _, recv in cps:
            recv().wait_recv()
        for send, _ in cps:
            send().wait_send()

    return _Rider(big, [jax.ShapeDtypeStruct(o.shape, o.dtype) for o in big], {i: i for i in range(n)},
                  [pltpu.SemaphoreType.DMA((n, 3)), pltpu.SemaphoreType.DMA((n, 3))], start, wait)


def _pair_exchange_rider(gs, halved):
    n = len(gs)

    def copies(r_in, r_out, sems):
        x, y, c = _mesh_pos()
        return [pltpu.make_async_remote_copy(
            src_ref=r_in[a].at[:, 1 - c] if halved else r_in[a], dst_ref=r_out[a], send_sem=sems[0].at[a],
            recv_sem=sems[1].at[a], device_id=(x, y, 1 - c), device_id_type=MESH) for a in range(n)]

    def start(r_in, r_out, sems):
        for cp in copies(r_in, r_out, sems):
            cp.start()

    def wait(r_in, r_out, sems):
        for cp in copies(r_in, r_out, sems):
            cp.wait()

    return _Rider(gs, [jax.ShapeDtypeStruct((g.shape[0],) + g.shape[-2:], g.dtype) for g in gs], {},
                  [pltpu.SemaphoreType.DMA((n,)), pltpu.SemaphoreType.DMA((n,))], start, wait)


def _chip_exchange_rider(to_send, by_chip, row_range=None):
    n = len(to_send)

    def copies(r_in, r_out, sems):
        x, y, c = _mesh_pos()
        me = 2 * x + y
        rows = (lambda ref: ref) if row_range is None else (lambda ref: ref.at[pl.ds(*row_range)])
        out = []
        for a in range(n):
            for k, (px, py) in enumerate(_other_chips(x, y)):
                send = functools.partial(
                    pltpu.make_async_remote_copy,
                    src_ref=rows(r_in[a].at[2 * px + py]), dst_ref=rows(r_out[a].at[me]),
                    send_sem=sems[0].at[a, k], recv_sem=sems[1].at[a, k], device_id=(px, py, c),
                    device_id_type=MESH)
                recv = functools.partial(
                    pltpu.make_async_remote_copy,
                    src_ref=rows(r_in[a].at[me]), dst_ref=rows(r_out[a].at[2 * px + py]),
                    send_sem=sems[0].at[a, k], recv_sem=sems[1].at[a, k], device_id=(px, py, c),
                    device_id_type=MESH)
                out.append((send, recv))
        return out

    def start(r_in, r_out, sems):
        for send, _ in copies(r_in, r_out, sems):
            send().start()

    def wait(r_in, r_out, sems):
        cps = copies(r_in, r_out, sems)
        for _, recv in cps:
            recv().wait_recv()
        for send, _ in cps:
            send().wait_send()

    return _Rider(list(to_send) + list(by_chip), [jax.ShapeDtypeStruct(b.shape, b.dtype) for b in by_chip],
                  {n + i: i for i in range(n)},
                  [pltpu.SemaphoreType.DMA((n, 3)), pltpu.SemaphoreType.DMA((n, 3))], start, wait)


HBM = pl.BlockSpec(memory_space=pltpu.HBM)
SEM = pl.BlockSpec(memory_space=pltpu.SEMAPHORE)


def _chip_exchange_start(to_send, by_chip, name):
    def body(send_ref, buf_ref, send_sems, recv_sems, send_thru, buf_thru, token):
        x, y, c = _mesh_pos()
        me = 2 * x + y
        for k, (px, py) in enumerate(_other_chips(x, y)):
            pltpu.make_async_remote_copy(
                src_ref=send_ref.at[2 * px + py], dst_ref=buf_ref.at[me], send_sem=send_sems.at[k],
                recv_sem=recv_sems.at[k], device_id=(px, py, c), device_id_type=MESH).start()
        token[...] = jnp.zeros_like(token)

    return _pinned_call(
        body, name=name,
        out_shape=(pltpu.SemaphoreType.DMA((3,)), pltpu.SemaphoreType.DMA((3,)),
                   pltpu.HBM(to_send.shape, to_send.dtype), pltpu.HBM(by_chip.shape, by_chip.dtype),
                   jax.ShapeDtypeStruct((8, LANES), F32)),
        in_specs=(HBM, HBM), out_specs=(SEM, SEM, HBM, HBM, pl.BlockSpec(memory_space=pltpu.VMEM)),
        input_output_aliases={0: 2, 1: 3},
        compiler_params=pltpu.CompilerParams(has_side_effects=pltpu.SideEffectType.DATAFLOW_SIDE_EFFECTING),
    )(pltpu.with_memory_space_constraint(to_send, pltpu.HBM), pltpu.with_memory_space_constraint(by_chip, pltpu.HBM))


def _chip_exchange_wait(send_sems, recv_sems, send_thru, buf_thru, after, name):
    n_after = len(after)

    def body(send_ref, buf_ref, send_sems, recv_sems, *rest):
        x, y, c = _mesh_pos()
        me = 2 * x + y
        for k, (px, py) in enumerate(_other_chips(x, y)):
            cp = pltpu.make_async_remote_copy(
                src_ref=send_ref.at[me], dst_ref=buf_ref.at[2 * px + py], send_sem=send_sems.at[k],
                recv_sem=recv_sems.at[k], device_id=(px, py, c), device_id_type=MESH)
            cp.wait_send()
            cp.wait_recv()

    return _pinned_call(
        body, name=name,
        out_shape=(pltpu.HBM(send_thru.shape, send_thru.dtype), pltpu.HBM(buf_thru.shape, buf_thru.dtype)),
        in_specs=(HBM, HBM, SEM, SEM) + (ANY,) * n_after, out_specs=(HBM, HBM), input_output_aliases={0: 0, 1: 1},
        compiler_params=pltpu.CompilerParams(has_side_effects=pltpu.SideEffectType.DATAFLOW_SIDE_EFFECTING),
    )(send_thru, buf_thru, send_sems, recv_sems, *after)[1]


def _pair_gather_rider(bufs):
    n = len(bufs)

    def copies(r_out, sems):
        x, y, c = _mesh_pos()
        out = []
        for a in range(n):
            send = functools.partial(
                    pltpu.make_async_remote_copy,
                src_ref=r_out[a].at[c], dst_ref=r_out[a].at[c], send_sem=sems[0].at[a],
                recv_sem=sems[1].at[a], device_id=(x, y, 1 - c), device_id_type=MESH)
            recv = functools.partial(
                    pltpu.make_async_remote_copy,
                src_ref=r_out[a].at[1 - c], dst_ref=r_out[a].at[1 - c], send_sem=sems[0].at[a],
                recv_sem=sems[1].at[a], device_id=(x, y, 1 - c), device_id_type=MESH)
            out.append((send, recv))
        return out

    def start(r_in, r_out, sems):
        for send, _ in copies(r_out, sems):
            send().start()

    def wait(r_in, r_out, sems):
        cps = copies(r_out, sems)
        for _, recv in cps:
            recv().wait_recv()
        for send, _ in cps:
            send().wait_send()

    return _Rider(bufs, [jax.ShapeDtypeStruct(b.shape, b.dtype) for b in bufs], {i: i for i in range(n)},
                  [pltpu.SemaphoreType.DMA((n,)), pltpu.SemaphoreType.DMA((n,))], start, wait)


def _add_own_half(g, recv, pos_arr, name):
    nb, rh, cols = g.shape[0], g.shape[-2], g.shape[-1]

    def body(pos_ref, g_ref, r_ref, send_ref, own_ref):
        s = (g_ref[...] + r_ref[...]).astype(BF16)
        send_ref[...] = s

        @pl.when(pl.program_id(0) == pos_ref[1])
        def _():
            own_ref[...] = s

    blk = pl.BlockSpec((None, rh, cols), lambda j, pos_ref: (j, 0, 0))
    g_spec = blk if g.ndim == 3 else pl.BlockSpec((None, None, rh, cols),
                                                   lambda j, pos_ref: (j, pos_ref[0], 0, 0))
    shape = jax.ShapeDtypeStruct((nb, rh, cols), BF16)
    return _pinned_call(
        body, name=name,
        grid_spec=pltpu.PrefetchScalarGridSpec(
            num_scalar_prefetch=1, grid=(nb,), in_specs=[g_spec, blk],
            out_specs=[blk, pl.BlockSpec((None, rh, cols), lambda j, pos_ref: (pos_ref[1], 0, 0))]),
        out_shape=[shape, shape], compiler_params=_params(),
    )(pos_arr, g, recv)


def _sum_chips(gath, pos_arr, name):
    nb, rh, cols = gath.shape

    def body(pos_ref, a_ref, b_ref, c_ref, d_ref, o_ref):
        del pos_ref
        o_ref[...] = ((a_ref[...].astype(F32) + b_ref[...].astype(F32)) + c_ref[...].astype(F32)) \
            + d_ref[...].astype(F32)

    tr = rh // 2 if (rh // 2) % 16 == 0 else rh
    specs = [pl.BlockSpec((None, tr, cols), functools.partial(lambda i, pos_ref, j: (j, i, 0), j=j))
             for j in range(nb)]
    return _pinned_call(
        body, name=name,
        grid_spec=pltpu.PrefetchScalarGridSpec(
            num_scalar_prefetch=1, grid=(rh // tr,), in_specs=specs,
            out_specs=pl.BlockSpec((None, tr, cols), lambda i, pos_ref: (pos_ref[0], i, 0))),
        out_shape=jax.ShapeDtypeStruct((2, rh, cols), F32), compiler_params=_params(),
    )(pos_arr, gath, gath, gath, gath)


def _small_allreduce(v, name, rider=None, after=()):
    n = v.shape[0]
    n_dev = 8

    def body(v_ref, o_ref, buf, send_sems, recv_sems):
        x, y, c = _mesh_pos()
        me = 4 * x + 2 * y + c
        buf[me] = v_ref[...]
        sends = []
        peers = []
        for r in range(1, n_dev):
            px = 1 - x if r & 4 else x
            py = 1 - y if r & 2 else y
            pc = 1 - c if r & 1 else c
            peers.append((px, py, pc))
            cp = pltpu.make_async_remote_copy(
                src_ref=v_ref, dst_ref=buf.at[me], send_sem=send_sems.at[r - 1],
                recv_sem=recv_sems.at[r - 1], device_id=(px, py, pc), device_id_type=MESH)
            cp.start()
            sends.append(cp)
        for r, (px, py, pc) in enumerate(peers):
            pltpu.make_async_remote_copy(
                src_ref=v_ref, dst_ref=buf.at[4 * px + 2 * py + pc], send_sem=send_sems.at[r],
                recv_sem=recv_sems.at[r], device_id=(px, py, pc), device_id_type=MESH).wait_recv()
        for cp in sends:
            cp.wait_send()
        acc = buf[0]
        for i in range(1, n_dev):
            acc = acc + buf[i]
        o_ref[...] = acc

    whole = pl.BlockSpec(v.shape, lambda i: (0, 0))
    return _pallas(
        body, name=name, grid=(1,), in_specs=[whole], out_specs=whole,
        out_shape=jax.ShapeDtypeStruct(v.shape, v.dtype), operands=[v],
        scratch_shapes=[pltpu.VMEM((n_dev, n, LANES), F32), pltpu.SemaphoreType.DMA((n_dev - 1,)),
                        pltpu.SemaphoreType.DMA((n_dev - 1,))],
        rider=rider, after=after)


def _adamw_math(w, g, m, v):
    m = ADAM_B1 * m + (1.0 - ADAM_B1) * g
    v = ADAM_B2 * v + (1.0 - ADAM_B2) * (g * g)
    m_hat = m / (1.0 - ADAM_B1 ** ADAM_STEP)
    v_hat = v / (1.0 - ADAM_B2 ** ADAM_STEP)
    delta = -ADAM_LR * (m_hat / (jnp.sqrt(v_hat) + ADAM_EPS) + ADAM_WD * w)
    return delta, m, v


def _adamw(w, g, m, v, name, after=()):
    r, c = w.shape
    tr = 128 if r % 128 == 0 else 64
    assert r % tr == 0

    def body(w_ref, g_ref, m_ref, v_ref, go_ref, d_ref, mo_ref, vo_ref):
        gv = g_ref[...]
        d, mn, vn = _adamw_math(w_ref[...], gv, m_ref[...], v_ref[...])
        go_ref[...] = gv
        d_ref[...] = d
        mo_ref[...] = mn
        vo_ref[...] = vn

    blk = pl.BlockSpec((tr, c), lambda i: (i, 0))
    return _pallas(body, name=name, grid=(r // tr,), in_specs=[blk] * 4, out_specs=[blk] * 4,
                   out_shape=[jax.ShapeDtypeStruct((r, c), F32)] * 4, operands=[w, g, m, v], after=after)


def _adamw_small(ws, gs, ms, vs, name):
    n = len(ws)

    def body(*refs):
        w_r, g_r, m_r, v_r = refs[:n], refs[n:2 * n], refs[2 * n:3 * n], refs[3 * n:4 * n]
        d_o, m_o, v_o = refs[4 * n:5 * n], refs[5 * n:6 * n], refs[6 * n:7 * n]
        for i in range(n):
            d, mn, vn = _adamw_math(w_r[i][...], g_r[i][...], m_r[i][...], v_r[i][...])
            d_o[i][...] = d
            m_o[i][...] = mn
            v_o[i][...] = vn

    specs = [pl.BlockSpec(w.shape, lambda i: (0, 0)) for w in ws]
    shapes = [jax.ShapeDtypeStruct(w.shape, F32) for w in ws]
    outs = pl.pallas_call(
        body, name=name, grid=(1,), in_specs=specs * 4, out_specs=specs * 3, out_shape=shapes * 3,
        compiler_params=_params(),
    )(*ws, *gs, *ms, *vs)
    return outs[:n], outs[n:2 * n], outs[2 * n:]


BIG = ("w_in", "w_o_attn", "w_pw_conv", "w_out", "w_ffn_in", "w_ffn_out")
ROW_SHARDED = ("w_out", "w_ffn_out")
SMALL = ("norm1_w", "b_gate", "q_norm_w", "k_norm_w", "conv_w", "conv_b", "conv_ln_w", "conv_ln_b", "norm2_w")
ORDER = ("norm1_w", "w_in", "b_gate", "q_norm_w", "k_norm_w", "w_o_attn", "conv_w", "conv_b", "conv_ln_w",
         "conv_ln_b", "w_pw_conv", "w_out", "norm2_w", "w_ffn_in", "w_ffn_out")
PACK_TILE = 8 * LANES


def _pack_small(parts):
    rows = []
    for p in parts:
        flat = p.reshape(-1)
        pad = (-flat.shape[0]) % PACK_TILE
        rows.append(jnp.pad(flat, (0, pad)).reshape(-1, LANES))
    return jnp.concatenate(rows, axis=0)


def _unpack_small(packed, shapes):
    out, row = [], 0
    for shp in shapes:
        size = int(np.prod(shp))
        nrow = -(-size // PACK_TILE) * (PACK_TILE // LANES)
        out.append(packed[row:row + nrow].reshape(-1)[:size].reshape(shp))
        row += nrow
    return out


def kernel(x, positions, norm1_w, w_in, b_gate, q_norm_w, k_norm_w, w_o_attn, conv_w, conv_b, conv_ln_w, conv_ln_b, w_pw_conv, w_out, norm2_w, w_ffn_in, w_ffn_out, loss_target, m_norm1_w, m_w_in, m_b_gate, m_q_norm_w, m_k_norm_w, m_w_o_attn, m_conv_w, m_conv_b, m_conv_ln_w, m_conv_ln_b, m_w_pw_conv, m_w_out, m_norm2_w, m_w_ffn_in, m_w_ffn_out, v_norm1_w, v_w_in, v_b_gate, v_q_norm_w, v_k_norm_w, v_w_o_attn, v_conv_w, v_conv_b, v_conv_ln_w, v_conv_ln_b, v_w_pw_conv, v_w_out, v_norm2_w, v_w_ffn_in, v_w_ffn_out):
    w = dict(norm1_w=norm1_w, w_in=w_in, b_gate=b_gate, q_norm_w=q_norm_w, k_norm_w=k_norm_w, w_o_attn=w_o_attn,
             conv_w=conv_w, conv_b=conv_b, conv_ln_w=conv_ln_w, conv_ln_b=conv_ln_b, w_pw_conv=w_pw_conv,
             w_out=w_out, norm2_w=norm2_w, w_ffn_in=w_ffn_in, w_ffn_out=w_ffn_out)
    m = dict(norm1_w=m_norm1_w, w_in=m_w_in, b_gate=m_b_gate, q_norm_w=m_q_norm_w, k_norm_w=m_k_norm_w,
             w_o_attn=m_w_o_attn, conv_w=m_conv_w, conv_b=m_conv_b, conv_ln_w=m_conv_ln_w,
             conv_ln_b=m_conv_ln_b, w_pw_conv=m_w_pw_conv, w_out=m_w_out, norm2_w=m_norm2_w,
             w_ffn_in=m_w_ffn_in, w_ffn_out=m_w_ffn_out)
    v = dict(norm1_w=v_norm1_w, w_in=v_w_in, b_gate=v_b_gate, q_norm_w=v_q_norm_w, k_norm_w=v_k_norm_w,
             w_o_attn=v_w_o_attn, conv_w=v_conv_w, conv_b=v_conv_b, conv_ln_w=v_conv_ln_w,
             conv_ln_b=v_conv_ln_b, w_pw_conv=v_w_pw_conv, w_out=v_w_out, norm2_w=v_norm2_w,
             w_ffn_in=v_w_ffn_in, w_ffn_out=v_w_ffn_out)
    cx, cy, cc = _mesh_pos()
    chip = 2 * cx + cy

    chip_arr = chip.reshape(1).astype(jnp.int32)
    pos_arr = jnp.stack([cc, chip]).astype(jnp.int32)
    bufs = {}
    for n in BIG:
        buf = _cast_into_slot(w[n][0], chip_arr, BF16, f"cast_{n}")
        bufs[n] = buf.reshape(N_CHIPS, 2, buf.shape[1] // 2, buf.shape[2])
    small_bufs = [_cast_into_slot(w[n][0], chip_arr, F32, f"slot_{n}") for n in ("conv_w", "b_gate")]
    w_in_buf, conv_w_buf, b_gate_buf = _allgather_inplace([bufs["w_in"]], small_bufs, "allgather_w_in")
    wts = dict(w_in=w_in_buf.reshape(N_CHIPS, -1, w_in_buf.shape[3]),
               conv_w=conv_w_buf.transpose(1, 0, 2).reshape(CONV_WIDTH, -1),
               b_gate=b_gate_buf.transpose(1, 0, 2).reshape(2, 1, -1),
               norm1_w=norm1_w, q_norm_w=q_norm_w, k_norm_w=k_norm_w, conv_b=conv_b, conv_ln_w=conv_ln_w,
               conv_ln_b=conv_ln_b, norm2_w=norm2_w)

    loss, grad_x, g, reduced, w_in_in_flight = _forward_backward(
        x[0], positions.reshape(-1, 1), loss_target[0], wts, [bufs[n] for n in LATE_GATHER], pos_arr)
    grads = {n: b.reshape(-1, b.shape[2]) for n, b in reduced.items()}

    *w_in_in_flight, started = w_in_in_flight
    delta, new_m, new_v = {}, {}, {}
    for n in EARLY_REDUCE:
        grads[n], delta[n], new_m[n], new_v[n] = _adamw(w[n][0], grads[n], m[n][0], v[n][0], f"adamw_{n}",
                                                        after=[started])
    small_parts = [loss] + [g[n] for n in SMALL]
    small_shapes = [p.shape for p in small_parts]
    summed = _small_allreduce(_pack_small(small_parts), "small_allreduce",
                              after=[delta[n] for n in EARLY_REDUCE])
    reduced = _unpack_small(summed, small_shapes)
    loss_total = reduced[0].reshape(())
    for n, r in zip(SMALL, reduced[1:]):
        grads[n] = r
    ch_shard = conv_w.shape[2]
    grads["conv_w"] = lax.dynamic_slice_in_dim(grads["conv_w"], chip * ch_shard, ch_shard, axis=1)
    d_shard = b_gate.shape[2]
    grads["b_gate"] = lax.dynamic_slice_in_dim(grads["b_gate"], chip * d_shard, d_shard, axis=1)

    by_chip_w_in = _chip_exchange_wait(*w_in_in_flight, after=[delta[n] for n in EARLY_REDUCE] + [summed],
                                       name="grads_w_in_exchange_wait")
    half_w_in = _sum_chips(by_chip_w_in, pos_arr, "grads_chip_sum_w_in")
    (shard_w_in,) = _comm_call(_pair_gather_rider([half_w_in]), "grads_pair_gather_w_in")
    grads["w_in"], delta["w_in"], new_m["w_in"], new_v["w_in"] = _adamw(
        w["w_in"][0], shard_w_in.reshape(-1, shard_w_in.shape[2]), m["w_in"][0], v["w_in"][0], "adamw_w_in")
    flat2 = lambda a: a.reshape(-1, a.shape[-1])
    d_s, m_s, v_s = _adamw_small([flat2(w[n]) for n in SMALL], [flat2(grads[n]) for n in SMALL],
                                 [flat2(m[n]) for n in SMALL], [flat2(v[n]) for n in SMALL], "adamw_small")
    for i, n in enumerate(SMALL):
        delta[n], new_m[n], new_v[n] = d_s[i], m_s[i], v_s[i]

    shaped = lambda d, n: d[n].reshape(w[n].shape)
    return (loss_total, grad_x[None], *[shaped(grads, n) for n in ORDER], *[shaped(delta, n) for n in ORDER],
            *[shaped(new_m, n) for n in ORDER], *[shaped(new_v, n) for n in ORDER])
```

```python
import functools

import numpy as np
import jax
import jax.numpy as jnp
from jax import lax
from jax.experimental import pallas as pl
from jax.experimental.pallas import tpu as pltpu

F32 = jnp.float32
BF16 = jnp.bfloat16
MESH = pl.DeviceIdType.MESH
ANY = pl.BlockSpec(memory_space=pl.ANY)

HEAD_DIM = 64
N_SLOT_HEADS = 8
DILATIONS = (1, 4, 16)
HALF_SPAN = 64
ROPE_THETA = 500000.0
ROT_DIM = 16
CONV_WIDTH = 31
EPS = 1e-6
NEG_INF = -1e30
ADAM_LR, ADAM_B1, ADAM_B2, ADAM_EPS, ADAM_WD, ADAM_STEP = 0.001, 0.9, 0.999, 1e-08, 0.01, 10

LANES = 128
QBLK = 128
KWIN = QBLK + 2 * HALF_SPAN
VMEM_LIMIT = 48 * 1024 * 1024
N_CHIPS = 4


def _params(**kw):
    return pltpu.CompilerParams(vmem_limit_bytes=VMEM_LIMIT, **kw)


class _Rider:
    def __init__(self, operands, out_shapes, aliases, scratch, start, wait):
        self.operands, self.out_shapes, self.aliases = list(operands), list(out_shapes), dict(aliases)
        self.scratch, self.start, self.wait = list(scratch), start, wait


def _pallas(body, *, name, grid, in_specs, out_specs, out_shape, operands, scratch_shapes=(), aliases=None,
            rider=None, after=()):
    single = not isinstance(out_specs, (list, tuple))
    out_specs_l = [out_specs] if single else list(out_specs)
    out_shape_l = [out_shape] if single else list(out_shape)
    aliases = dict(aliases or {})

    def call(fn, all_in_specs, all_out_specs, all_out_shape, all_scratch, all_aliases, all_operands):
        return pl.pallas_call(
            fn, name=name, grid=grid, in_specs=all_in_specs, out_specs=all_out_specs, out_shape=all_out_shape,
            scratch_shapes=all_scratch, input_output_aliases=all_aliases, compiler_params=_params(),
        )(*all_operands)

    if rider is None:
        n_main = len(in_specs)

        def ordered(*refs):
            body(*refs[:n_main], *refs[n_main + len(after):])

        res = call(ordered if after else body, list(in_specs) + [ANY] * len(after), out_specs_l, out_shape_l,
                   list(scratch_shapes), aliases, list(operands) + list(after))
        return res[0] if single else res
    assert not after
    n_in, n_rin = len(in_specs), len(rider.operands)
    n_out, n_rout = len(out_specs_l), len(rider.out_shapes)
    n_sc = len(scratch_shapes)

    def wrapped(*refs):
        main_in, r_in = refs[:n_in], refs[n_in:n_in + n_rin]
        o0 = n_in + n_rin
        main_out, r_out = refs[o0:o0 + n_out], refs[o0 + n_out:o0 + n_out + n_rout]
        s0 = o0 + n_out + n_rout
        main_sc, r_sc = refs[s0:s0 + n_sc], refs[s0 + n_sc:]
        ids = [pl.program_id(d) for d in range(len(grid))]
        first = functools.reduce(jnp.logical_and, [i == 0 for i in ids])
        last = functools.reduce(jnp.logical_and, [i == n - 1 for i, n in zip(ids, grid)])

        @pl.when(first)
        def _():
            rider.start(r_in, r_out, r_sc)

        body(*main_in, *main_out, *main_sc)

        @pl.when(last)
        def _():
            rider.wait(r_in, r_out, r_sc)

    for src, dst in rider.aliases.items():
        aliases[n_in + src] = n_out + dst
    res = call(wrapped, list(in_specs) + [ANY] * n_rin, out_specs_l + [ANY] * n_rout,
               out_shape_l + rider.out_shapes, list(scratch_shapes) + rider.scratch, aliases,
               list(operands) + rider.operands)
    main = res[:n_out]
    return (main[0] if single else main), res[n_out:]


def _matmul(a, b, *, mode, tm, tn, tk, out_dtype, name, b_blocked=False,
            out_blocked=None, rider=None, after=()):
    a_shape = a.shape
    if mode == "nn":
        m_dim, k_dim = a_shape
        n_dim = b.shape[0] * b.shape[2] if b_blocked else b.shape[1]
        rows, cols, red = m_dim, n_dim, k_dim
    elif mode == "nt":
        m_dim, n_dim = a_shape
        k_dim = b.shape[1] if b_blocked else b.shape[0]
        rows, cols, red = m_dim, k_dim, n_dim
    else:
        m_dim, k_dim = a_shape
        n_dim = b.shape[1]
        rows, cols, red = k_dim, n_dim, m_dim
    assert rows % tm == 0 and cols % tn == 0 and red % tk == 0, (name, rows, cols, red)
    ni, nj, nk = rows // tm, cols // tn, red // tk

    if mode == "nn":
        a_spec = pl.BlockSpec((tm, tk), lambda i, j, k: (i, k))
        if b_blocked:
            per = b.shape[2] // tn
            b_spec = pl.BlockSpec((None, tk, tn), lambda i, j, k: (j // per, k, j % per))
        else:
            b_spec = pl.BlockSpec((tk, tn), lambda i, j, k: (k, j))
        dims = (((1,), (0,)), ((), ()))
    elif mode == "nt":
        a_spec = pl.BlockSpec((tm, tk), lambda i, j, k: (i, k))
        if b_blocked:
            per = b.shape[2] // tk
            b_spec = pl.BlockSpec((None, tn, tk), lambda i, j, k: (k // per, j, k % per))
        else:
            b_spec = pl.BlockSpec((tn, tk), lambda i, j, k: (j, k))
        dims = (((1,), (1,)), ((), ()))
    else:
        a_spec = pl.BlockSpec((tk, tm), lambda i, j, k: (k, i))
        b_spec = pl.BlockSpec((tk, tn), lambda i, j, k: (k, j))
        dims = (((0,), (0,)), ((), ()))

    if out_blocked:
        per_o = (cols // out_blocked) // tn
        out_spec = pl.BlockSpec((None, tm, tn), lambda i, j, k: (j // per_o, i, j % per_o))
        out_shape = jax.ShapeDtypeStruct((out_blocked, rows, cols // out_blocked), out_dtype)
    else:
        out_spec = pl.BlockSpec((tm, tn), lambda i, j, k: (i, j))
        out_shape = jax.ShapeDtypeStruct((rows, cols), out_dtype)

    def body(a_ref, b_ref, o_ref, *acc):
        prod = lax.dot_general(a_ref[...], b_ref[...], dims, preferred_element_type=F32)
        if nk == 1:
            o_ref[...] = prod.astype(out_dtype)
        else:
            acc_ref, = acc
            k = pl.program_id(2)

            @pl.when(k == 0)
            def _():
                acc_ref[...] = prod

            @pl.when(k > 0)
            def _():
                acc_ref[...] += prod

            @pl.when(k == nk - 1)
            def _():
                o_ref[...] = acc_ref[...].astype(out_dtype)

    scratch = [pltpu.VMEM((tm, tn), F32)] if nk > 1 else []
    return _pallas(body, name=name, grid=(ni, nj, nk), in_specs=[a_spec, b_spec], out_specs=out_spec,
                   out_shape=out_shape, operands=[a, b], scratch_shapes=scratch, rider=rider, after=after)


def _rmsnorm_fwd(x, w, name):
    s, d = x.shape
    tm = 256

    def body(x_ref, w_ref, o_ref):
        xv = x_ref[...]
        rstd = lax.rsqrt(jnp.mean(xv * xv, axis=-1, keepdims=True) + EPS)
        o_ref[...] = (xv * rstd * w_ref[...]).astype(BF16)

    return pl.pallas_call(
        body, name=name, grid=(s // tm,),
        in_specs=[pl.BlockSpec((tm, d), lambda i: (i, 0)), pl.BlockSpec((1, d), lambda i: (0, 0))],
        out_specs=pl.BlockSpec((tm, d), lambda i: (i, 0)),
        out_shape=jax.ShapeDtypeStruct((s, d), BF16), compiler_params=_params(),
    )(x, w)


def _rmsnorm_bwd(dh, x, w, dres, name, rider=None):
    s, d = x.shape
    tm = 256

    def body(dh_ref, x_ref, w_ref, dres_ref, dx_ref, dxb_ref, dw_ref):
        xv = x_ref[...]
        rstd = lax.rsqrt(jnp.mean(xv * xv, axis=-1, keepdims=True) + EPS)
        xhat = xv * rstd
        dhv = dh_ref[...]
        g = dhv * w_ref[...]
        dx = rstd * (g - xhat * jnp.mean(g * xhat, axis=-1, keepdims=True)) + dres_ref[...]
        dx_ref[...] = dx
        dxb_ref[...] = dx.astype(BF16)
        part = jnp.sum(dhv * xhat, axis=0, keepdims=True)

        @pl.when(pl.program_id(0) == 0)
        def _():
            dw_ref[...] = part

        @pl.when(pl.program_id(0) > 0)
        def _():
            dw_ref[...] += part

    row = pl.BlockSpec((tm, d), lambda i: (i, 0))
    vec = pl.BlockSpec((1, d), lambda i: (0, 0))
    return _pallas(
        body, name=name, grid=(s // tm,), in_specs=[row, row, vec, row], out_specs=[row, row, vec],
        out_shape=[jax.ShapeDtypeStruct((s, d), F32), jax.ShapeDtypeStruct((s, d), BF16),
                   jax.ShapeDtypeStruct((1, d), F32)],
        operands=[dh, x, w, dres], rider=rider)


def _rope_consts():
    lane = np.arange(LANES)
    in_head = lane % HEAD_DIM
    inv_freq = ROPE_THETA ** (-jnp.arange(0, ROT_DIM, 2, dtype=F32) / ROT_DIM)
    invf = jnp.where(jnp.asarray(in_head < ROT_DIM), jnp.tile(inv_freq, LANES // (ROT_DIM // 2)), 0.0)
    m_a = np.where(in_head < ROT_DIM // 2, -1.0, 0.0).astype(np.float32)
    m_b = np.where((in_head >= ROT_DIM // 2) & (in_head < ROT_DIM), 1.0, 0.0).astype(np.float32)
    block_diag = (lane[:, None] // HEAD_DIM == lane[None, :] // HEAD_DIM).astype(np.float32)
    return (invf.reshape(1, LANES).astype(F32), jnp.asarray(m_a).reshape(1, LANES),
            jnp.asarray(m_b).reshape(1, LANES), jnp.asarray(block_diag, dtype=BF16))


def _head_sums(v, bd):
    hi = v.astype(BF16)
    lo = (v - hi.astype(F32)).astype(BF16)
    return jnp.dot(hi, bd, preferred_element_type=F32) + jnp.dot(lo, bd, preferred_element_type=F32)


def _qk_fwd(proj, pos_col, qw2, kw2, consts, name, rider=None):
    s = proj.shape[0]
    width = 3 * N_SLOT_HEADS * HEAD_DIM
    tm = 128
    invf, m_a, m_b, bd = consts
    scale = HEAD_DIM ** -0.5

    def body(q_ref, k_ref, pos_ref, qw_ref, kw_ref, invf_ref, ma_ref, mb_ref, bd_ref, qo_ref, ko_ref):
        ang = pos_ref[...].astype(F32) * invf_ref[...]
        cos = jnp.cos(ang)
        sin = jnp.sin(ang)
        s_a = sin * ma_ref[...]
        s_b = sin * mb_ref[...]
        bdv = bd_ref[...]
        for src, w_ref, dst, sc in ((q_ref, qw_ref, qo_ref, scale), (k_ref, kw_ref, ko_ref, 1.0)):
            for cb in range(width // LANES):
                cols = slice(cb * LANES, (cb + 1) * LANES)
                t = src[:, cols]
                rstd = lax.rsqrt(_head_sums(t * t, bdv) * (1.0 / HEAD_DIM) + EPS)
                y = t * rstd * w_ref[...]
                r = y * cos + pltpu.roll(y, LANES - 8, axis=1) * s_a + pltpu.roll(y, 8, axis=1) * s_b
                dst[:, cols] = r * sc if sc != 1.0 else r

    vec = pl.BlockSpec((1, LANES), lambda i: (0, 0))
    return _pallas(
        body, name=name, grid=(s // tm,),
        in_specs=[pl.BlockSpec((tm, width), lambda i: (i, 0)), pl.BlockSpec((tm, width), lambda i: (i, 1)),
                  pl.BlockSpec((tm, 1), lambda i: (i, 0)), vec, vec, vec, vec, vec,
                  pl.BlockSpec((LANES, LANES), lambda i: (0, 0))],
        out_specs=[pl.BlockSpec((tm, width), lambda i: (i, 0))] * 2,
        out_shape=[jax.ShapeDtypeStruct((s, width), F32)] * 2,
        operands=[proj, proj, pos_col, qw2, kw2, invf, m_a, m_b, bd], rider=rider)


def _qk_bwd(dqn, dkn, dv, da, db, dgl, proj, pos_col, qw2, kw2, consts, name, rider=None):
    s = proj.shape[0]
    width = 3 * N_SLOT_HEADS * HEAD_DIM
    ch = da.shape[1]
    gate_w = dgl.shape[1]
    out_w = 3 * width + 2 * ch + gate_w
    assert out_w == proj.shape[1]
    tm = 128
    invf, m_a, m_b, bd = consts
    scale = HEAD_DIM ** -0.5

    def body(dq_ref, dk_ref, dv_ref, da_ref, db_ref, dgl_ref, q_ref, k_ref, pos_ref, qw_ref, kw_ref,
             invf_ref, ma_ref, mb_ref, bd_ref, out_ref, dqw_ref, dkw_ref):
        ang = pos_ref[...].astype(F32) * invf_ref[...]
        cos = jnp.cos(ang)
        sin = jnp.sin(ang)
        s_a = sin * ma_ref[...]
        s_b = sin * mb_ref[...]
        bdv = bd_ref[...]
        first = pl.program_id(0) == 0
        for src, dsrc, w_ref, col0, dw_ref, sc in ((q_ref, dq_ref, qw_ref, 0, dqw_ref, scale),
                                                   (k_ref, dk_ref, kw_ref, width, dkw_ref, 1.0)):
            dw_acc = jnp.zeros((1, LANES), F32)
            for cb in range(width // LANES):
                cols = slice(cb * LANES, (cb + 1) * LANES)
                t = src[:, cols]
                dr = dsrc[:, cols]
                if sc != 1.0:
                    dr = dr * sc
                dy = dr * cos + pltpu.roll(dr * s_a, 8, axis=1) + pltpu.roll(dr * s_b, LANES - 8, axis=1)
                rstd = lax.rsqrt(_head_sums(t * t, bdv) * (1.0 / HEAD_DIM) + EPS)
                xhat = t * rstd
                g = dy * w_ref[...]
                dt = rstd * (g - xhat * (_head_sums(g * xhat, bdv) * (1.0 / HEAD_DIM)))
                out_ref[:, col0 + cb * LANES: col0 + (cb + 1) * LANES] = dt.astype(BF16)
                dw_acc = dw_acc + jnp.sum(dy * xhat, axis=0, keepdims=True)
            dw_acc = dw_acc + pltpu.roll(dw_acc, HEAD_DIM, axis=1)

            @pl.when(first)
            def _(dw_ref=dw_ref, dw_acc=dw_acc):
                dw_ref[...] = dw_acc

            @pl.when(jnp.logical_not(first))
            def _(dw_ref=dw_ref, dw_acc=dw_acc):
                dw_ref[...] += dw_acc
        out_ref[:, 2 * width: 3 * width] = dv_ref[...].astype(BF16)
        out_ref[:, 3 * width: 3 * width + ch] = da_ref[...]
        out_ref[:, 3 * width + ch: 3 * width + 2 * ch] = db_ref[...]
        out_ref[:, 3 * width + 2 * ch: out_w] = dgl_ref[...]

    vec = pl.BlockSpec((1, LANES), lambda i: (0, 0))
    blk = lambda c: pl.BlockSpec((tm, width), lambda i: (i, c))
    cblk = pl.BlockSpec((tm, ch), lambda i: (i, 0))
    return _pallas(
        body, name=name, grid=(s // tm,),
        in_specs=[blk(0), blk(0), blk(0), cblk, cblk, pl.BlockSpec((tm, gate_w), lambda i: (i, 0)),
                  blk(0), blk(1), pl.BlockSpec((tm, 1), lambda i: (i, 0)), vec, vec, vec, vec, vec,
                  pl.BlockSpec((LANES, LANES), lambda i: (0, 0))],
        out_specs=[pl.BlockSpec((tm, out_w), lambda i: (i, 0)), vec, vec],
        out_shape=[jax.ShapeDtypeStruct((s, out_w), BF16)] + [jax.ShapeDtypeStruct((1, LANES), F32)] * 2,
        operands=[dqn, dkn, dv, da, db, dgl, proj, proj, pos_col, qw2, kw2, invf, m_a, m_b, bd],
        rider=rider)


def _row_chunks(n_rows, fn, chunk=256):
    def step(i, c):
        fn(pl.ds(pl.multiple_of(i * chunk, chunk), chunk))
        return c
    lax.fori_loop(0, n_rows // chunk, step, 0)


def _to_residue_major(dst, src, s, d, dst_off=0, cast=None):
    seq = s // d
    for r in range(d):
        v = src[...] if d == 1 else src[pl.ds(r, seq, stride=d), :]
        dst[dst_off + r * seq: dst_off + (r + 1) * seq, :] = v if cast is None else v.astype(cast)


def _from_residue_major(dst, src, s, d, src_off=0):
    seq = s // d
    for r in range(d):
        v = src[src_off + r * seq: src_off + (r + 1) * seq, :]
        if d == 1:
            dst[...] = v
        else:
            dst[pl.ds(r, seq, stride=d), :] = v


def _band_bias():
    qi = lax.broadcasted_iota(jnp.int32, (QBLK, KWIN), 0)
    kj = lax.broadcasted_iota(jnp.int32, (QBLK, KWIN), 1)
    return jnp.where(jnp.abs(kj - HALF_SPAN - qi) <= HALF_SPAN, 0.0, NEG_INF).astype(F32)


def _range_bias(base, seq):
    kj = lax.broadcasted_iota(jnp.int32, (1, KWIN), 1)
    lo = (base & -seq) - base + HALF_SPAN
    return jnp.where((kj >= lo) & (kj < lo + seq), 0.0, NEG_INF).astype(F32)


ATTN_BLOCKS_PER_TRIP = 4


def _skewed_blocks(n_blk, produce, consume):
    per = ATTN_BLOCKS_PER_TRIP
    produce(0, 0)

    def trip(i, carry):
        for u in range(per):
            consume(per * i + u, u % 2)
            produce(per * i + u + 1, (u + 1) % 2)
        return carry

    n_trips = n_blk // per - 1
    lax.fori_loop(0, n_trips, trip, 0)
    for b in range(per * n_trips, n_blk):
        consume(b, b % 2)
        if b + 1 < n_blk:
            produce(b + 1, (b + 1) % 2)


def _block_base(b):
    return b * QBLK if isinstance(b, int) else pl.multiple_of(b * QBLK, QBLK)


def _attn_fwd(qn, kn, proj, name, rider=None):
    s = qn.shape[0]
    n_pairs = N_SLOT_HEADS * HEAD_DIM // LANES
    v_col0 = 2 * qn.shape[1] // LANES
    nt_dims = (((1,), (1,)), ((), ()))

    def body(q_ref, k_ref, v_ref, attn_ref, lse_ref, attn_b_ref, q_rm, k_rm, v_rm, acc_rm, m_rm, l_rm,
             acc_p, m_p, l_p, m_run, l_run, acc_run, band, s_buf, m_buf):
        g = pl.program_id(1)
        zpad = jnp.zeros((HALF_SPAN, LANES), BF16)
        k_rm[0:HALF_SPAN, :] = zpad
        k_rm[s + HALF_SPAN: s + 2 * HALF_SPAN, :] = zpad
        v_rm[0:HALF_SPAN, 0:LANES] = zpad
        v_rm[s + HALF_SPAN: s + 2 * HALF_SPAN, 0:LANES] = zpad

        def ones_rows(rows):
            v_rm[pl.ds(rows.start, rows.size), LANES:2 * LANES] = jnp.ones((rows.size, LANES), BF16)

        _row_chunks(s + 2 * HALF_SPAN, ones_rows, chunk=2 * HALF_SPAN)
        band[...] = _band_bias()
        lane = lax.broadcasted_iota(jnp.int32, (QBLK, LANES), 1)
        low = lane < HEAD_DIM
        n_blk = s // QBLK

        for gi, d in enumerate(DILATIONS):
            @pl.when(g == gi)
            def _(gi=gi, d=d):
                seq = s // d
                _to_residue_major(q_rm, q_ref, s, d, cast=BF16)
                _to_residue_major(k_rm, k_ref, s, d, dst_off=HALF_SPAN, cast=BF16)
                _to_residue_major(v_rm.at[:, 0:LANES], v_ref, s, d, dst_off=HALF_SPAN, cast=BF16)

                def scores(b, slot):
                    base = _block_base(b)
                    q = q_rm[pl.ds(base, QBLK), :]
                    zero = jnp.zeros_like(q)
                    q2 = jnp.concatenate([jnp.where(low, q, zero), jnp.where(low, zero, q)], axis=0)
                    sc = lax.dot_general(q2, k_rm[pl.ds(base, KWIN), :], nt_dims, preferred_element_type=F32)
                    bias = band[...] + _range_bias(base, seq)
                    for hh in range(2):
                        rows = slice(hh * QBLK, (hh + 1) * QBLK)
                        sh = sc[rows, :] + bias
                        s_buf[slot, rows, :] = sh
                        m_buf[slot, rows, :] = jnp.broadcast_to(jnp.max(sh, axis=-1, keepdims=True), (QBLK, LANES))

                def outputs(b, slot):
                    base = _block_base(b)
                    sv = s_buf[slot]
                    mb = m_buf[slot]
                    p = jnp.exp(jnp.concatenate([sv[:, 0:LANES] - mb, sv[:, LANES:2 * LANES] - mb], axis=1))
                    pv = jnp.dot(p.astype(BF16), v_rm[pl.ds(base, KWIN), :], preferred_element_type=F32)
                    rows = pl.ds(base, QBLK)
                    acc_rm[rows, :] = jnp.where(low, pv[0:QBLK, 0:LANES], pv[QBLK:2 * QBLK, 0:LANES])
                    l_rm[rows, :] = jnp.where(low, pv[0:QBLK, LANES:2 * LANES], pv[QBLK:2 * QBLK, LANES:2 * LANES])
                    m_rm[rows, :] = jnp.where(low, mb[0:QBLK, :], mb[QBLK:2 * QBLK, :])

                _skewed_blocks(n_blk, scores, outputs)
                if d == 1:
                    src = (acc_rm, m_rm, l_rm)
                else:
                    for dst_, src_ in ((acc_p, acc_rm), (m_p, m_rm), (l_p, l_rm)):
                        _from_residue_major(dst_, src_, s, d)
                    src = (acc_p, m_p, l_p)

                def combine(rows):
                    a_g, m_g, l_g = src[0][rows, :], src[1][rows, :], src[2][rows, :]
                    if gi == 0:
                        m_new, l_new, a_new = m_g, l_g, a_g
                    else:
                        m_old = m_run[rows, :]
                        m_new = jnp.maximum(m_old, m_g)
                        w_old = jnp.exp(m_old - m_new)
                        w_g = jnp.exp(m_g - m_new)
                        l_new = l_run[rows, :] * w_old + l_g * w_g
                        a_new = acc_run[rows, :] * w_old + a_g * w_g
                    if gi == len(DILATIONS) - 1:
                        out = a_new / l_new
                        attn_ref[rows, :] = out
                        attn_b_ref[rows, :] = out.astype(BF16)
                        lse_ref[rows, :] = m_new + jnp.log(l_new)
                    else:
                        m_run[rows, :] = m_new
                        l_run[rows, :] = l_new
                        acc_run[rows, :] = a_new

                _row_chunks(s, combine)

    qk_spec = pl.BlockSpec((s, LANES), lambda hp, g: (0, g * n_pairs + hp))
    v_spec = pl.BlockSpec((s, LANES), lambda hp, g: (0, v_col0 + g * n_pairs + hp))
    o_spec = pl.BlockSpec((s, LANES), lambda hp, g: (0, hp))
    f32buf = pltpu.VMEM((s, LANES), F32)
    return _pallas(
        body, name=name, grid=(n_pairs, len(DILATIONS)), in_specs=[qk_spec, qk_spec, v_spec],
        out_specs=[o_spec, o_spec, o_spec],
        out_shape=[jax.ShapeDtypeStruct((s, n_pairs * LANES), F32)] * 2
        + [jax.ShapeDtypeStruct((s, n_pairs * LANES), BF16)],
        operands=[qn, kn, proj],
        scratch_shapes=[pltpu.VMEM((s, LANES), BF16), pltpu.VMEM((s + 2 * HALF_SPAN, LANES), BF16),
                        pltpu.VMEM((s + 2 * HALF_SPAN, 2 * LANES), BF16)] + [f32buf] * 9
        + [pltpu.VMEM((QBLK, KWIN), F32), pltpu.VMEM((2, 2 * QBLK, KWIN), F32),
           pltpu.VMEM((2, 2 * QBLK, LANES), F32)],
        rider=rider)


def _attn_bwd(qn, kn, proj, dattn, attn, lse, bd, name, rider=None):
    s = qn.shape[0]
    n_pairs = N_SLOT_HEADS * HEAD_DIM // LANES
    v_col0 = 2 * qn.shape[1] // LANES
    nt_dims = (((1,), (1,)), ((), ()))
    tn_dims = (((0,), (0,)), ((), ()))
    spad = s + 2 * HALF_SPAN

    def body(q_ref, k_ref, v_ref, do_ref, o_ref, lse_ref, bd_ref, dq_ref, dk_ref, dv_ref,
             q_rm, k_rm, v_rm, do_rm, lse0_rm, lse1_rm, dd0_rm, dd1_rm, dq_rm, dk_rm, dv_rm,
             lse0_p, lse1_p, dd0_p, dd1_p, band, p_buf, ds_buf):
        g = pl.program_id(1)
        zpad = jnp.zeros((HALF_SPAN, LANES), BF16)
        for buf in (k_rm, v_rm):
            buf[0:HALF_SPAN, :] = zpad
            buf[s + HALF_SPAN: spad, :] = zpad
        zf = jnp.zeros((HALF_SPAN, LANES), F32)
        for buf in (dk_rm, dv_rm):
            buf[0:HALF_SPAN, :] = zf
            buf[s + HALF_SPAN: spad, :] = zf
        band[...] = _band_bias()

        def clear(rows):
            z = jnp.zeros((rows.size, LANES), F32)
            dk_rm[pl.ds(rows.start + HALF_SPAN, rows.size), :] = z
            dv_rm[pl.ds(rows.start + HALF_SPAN, rows.size), :] = z

        _row_chunks(s, clear)

        def prepare(rows):
            lo = lax.broadcasted_iota(jnp.int32, (rows.size, LANES), 1) < HEAD_DIM
            dsum = _head_sums(do_ref[rows, :] * o_ref[rows, :], bd_ref[...])
            dswap = pltpu.roll(dsum, HEAD_DIM, axis=1)
            dd0_p[rows, :] = jnp.where(lo, dsum, dswap)
            dd1_p[rows, :] = jnp.where(lo, dswap, dsum)
            lv = lse_ref[rows, :]
            lswap = pltpu.roll(lv, HEAD_DIM, axis=1)
            lse0_p[rows, :] = jnp.where(lo, lv, lswap)
            lse1_p[rows, :] = jnp.where(lo, lswap, lv)

        @pl.when(g == 0)
        def _():
            _row_chunks(s, prepare)
        lane = lax.broadcasted_iota(jnp.int32, (QBLK, LANES), 1)
        low = lane < HEAD_DIM
        n_blk = s // QBLK

        def stacked(ref, rows):
            val = ref[rows, :]
            zero = jnp.zeros_like(val)
            return jnp.concatenate([jnp.where(low, val, zero), jnp.where(low, zero, val)], axis=0)

        for gi, d in enumerate(DILATIONS):
            @pl.when(g == gi)
            def _(d=d):
                seq = s // d
                _to_residue_major(q_rm, q_ref, s, d, cast=BF16)
                _to_residue_major(k_rm, k_ref, s, d, dst_off=HALF_SPAN, cast=BF16)
                _to_residue_major(v_rm, v_ref, s, d, dst_off=HALF_SPAN, cast=BF16)
                _to_residue_major(do_rm, do_ref, s, d, cast=BF16)
                for dst_, src_ in ((lse0_rm, lse0_p), (lse1_rm, lse1_p), (dd0_rm, dd0_p), (dd1_rm, dd1_p)):
                    _to_residue_major(dst_, src_, s, d)

                def scores(b, slot):
                    base = _block_base(b)
                    rows = pl.ds(base, QBLK)
                    win = pl.ds(base, KWIN)
                    sc = lax.dot_general(stacked(q_rm, rows), k_rm[win, :], nt_dims, preferred_element_type=F32)
                    dp = lax.dot_general(stacked(do_rm, rows), v_rm[win, :], nt_dims, preferred_element_type=F32)
                    bias = band[...] + _range_bias(base, seq)
                    for hh, (lse_r, dd_r) in enumerate(((lse0_rm, dd0_rm), (lse1_rm, dd1_rm))):
                        r = slice(hh * QBLK, (hh + 1) * QBLK)
                        lse_h = lse_r[rows, :]
                        dd_h = dd_r[rows, :]
                        sh = sc[r, :] + bias
                        p = jnp.exp(jnp.concatenate([sh[:, 0:LANES] - lse_h, sh[:, LANES:KWIN] - lse_h], axis=1))
                        dph = dp[r, :]
                        ds = p * jnp.concatenate([dph[:, 0:LANES] - dd_h, dph[:, LANES:KWIN] - dd_h], axis=1)
                        p_buf[slot, r, :] = p.astype(BF16)
                        ds_buf[slot, r, :] = ds.astype(BF16)

                def grads(b, slot):
                    base = _block_base(b)
                    rows = pl.ds(base, QBLK)
                    win = pl.ds(base, KWIN)
                    p = p_buf[slot]
                    ds = ds_buf[slot]
                    dq2 = jnp.dot(ds, k_rm[win, :], preferred_element_type=F32)
                    dq_rm[rows, :] = jnp.where(low, dq2[0:QBLK, :], dq2[QBLK:2 * QBLK, :])
                    dk_rm[win, :] += lax.dot_general(ds, stacked(q_rm, rows), tn_dims, preferred_element_type=F32)
                    dv_rm[win, :] += lax.dot_general(p, stacked(do_rm, rows), tn_dims, preferred_element_type=F32)

                _skewed_blocks(n_blk, scores, grads)
                _from_residue_major(dq_ref, dq_rm, s, d)
                _from_residue_major(dk_ref, dk_rm, s, d, src_off=HALF_SPAN)
                _from_residue_major(dv_ref, dv_rm, s, d, src_off=HALF_SPAN)

    qk_spec = pl.BlockSpec((s, LANES), lambda hp, g: (0, g * n_pairs + hp))
    v_spec = pl.BlockSpec((s, LANES), lambda hp, g: (0, v_col0 + g * n_pairs + hp))
    o_spec = pl.BlockSpec((s, LANES), lambda hp, g: (0, hp))
    width = qn.shape[1]
    f32buf = pltpu.VMEM((s, LANES), F32)
    f32pad = pltpu.VMEM((spad, LANES), F32)
    return _pallas(
        body, name=name, grid=(n_pairs, len(DILATIONS)),
        in_specs=[qk_spec, qk_spec, v_spec, o_spec, o_spec, o_spec,
                  pl.BlockSpec((LANES, LANES), lambda hp, g: (0, 0))],
        out_specs=[qk_spec, qk_spec, qk_spec],
        out_shape=[jax.ShapeDtypeStruct((s, width), F32)] * 3,
        operands=[qn, kn, proj, dattn, attn, lse, bd],
        scratch_shapes=[pltpu.VMEM((s, LANES), BF16), pltpu.VMEM((spad, LANES), BF16),
                        pltpu.VMEM((spad, LANES), BF16), pltpu.VMEM((s, LANES), BF16),
                        f32buf, f32buf, f32buf, f32buf, f32buf, f32pad, f32pad,
                        f32buf, f32buf, f32buf, f32buf, pltpu.VMEM((QBLK, KWIN), F32),
                        pltpu.VMEM((2, 2 * QBLK, KWIN), BF16), pltpu.VMEM((2, 2 * QBLK, KWIN), BF16)],
        rider=rider)


CONV_PAD = 16


def _conv_fwd(proj, conv_w, conv_b, col0, name, rider=None):
    s = proj.shape[0]
    ch = conv_w.shape[1]
    nblk = ch // LANES
    a0 = col0 // LANES
    tr = 256
    shift = CONV_PAD - (CONV_WIDTH - 1) // 2

    def body(a_ref, b_ref, w_ref, bias_ref, u0_ref, uc_ref, pad):
        z = jnp.zeros((CONV_PAD, LANES), F32)
        pad[0:CONV_PAD, :] = z
        pad[s + CONV_PAD: s + 2 * CONV_PAD, :] = z

        def glu(rows):
            u0 = a_ref[rows, :] * jax.nn.sigmoid(b_ref[rows, :])
            u0_ref[rows, :] = u0
            pad[pl.ds(rows.start + CONV_PAD, rows.size), :] = u0

        _row_chunks(s, glu)
        for t in range(0, s, tr):
            acc = jnp.broadcast_to(bias_ref[...], (tr, LANES))
            for k in range(CONV_WIDTH):
                acc = acc + w_ref[k:k + 1, :] * pad[t + k + shift: t + k + shift + tr, :]
            uc_ref[t:t + tr, :] = acc

    return _pallas(
        body, name=name, grid=(nblk,),
        in_specs=[pl.BlockSpec((s, LANES), lambda c: (0, a0 + c)),
                  pl.BlockSpec((s, LANES), lambda c: (0, a0 + nblk + c)),
                  pl.BlockSpec((CONV_WIDTH, LANES), lambda c: (0, c)),
                  pl.BlockSpec((1, LANES), lambda c: (0, c))],
        out_specs=[pl.BlockSpec((s, LANES), lambda c: (0, c))] * 2,
        out_shape=[jax.ShapeDtypeStruct((s, ch), F32)] * 2, operands=[proj, proj, conv_w, conv_b],
        scratch_shapes=[pltpu.VMEM((s + 2 * CONV_PAD, LANES), F32)], rider=rider)


def _ln_silu_fwd(uc, ln_w, ln_b, name):
    s, ch = uc.shape
    tm = 256

    def body(u_ref, w_ref, b_ref, o_ref):
        u = u_ref[...]
        mu = jnp.mean(u, axis=-1, keepdims=True)
        xc = u - mu
        rstd = lax.rsqrt(jnp.mean(xc * xc, axis=-1, keepdims=True) + EPS)
        z = xc * rstd * w_ref[...] + b_ref[...]
        o_ref[...] = (z * jax.nn.sigmoid(z)).astype(BF16)

    row = pl.BlockSpec((tm, ch), lambda i: (i, 0))
    vec = pl.BlockSpec((1, ch), lambda i: (0, 0))
    return pl.pallas_call(
        body, name=name, grid=(s // tm,), in_specs=[row, vec, vec], out_specs=row,
        out_shape=jax.ShapeDtypeStruct((s, ch), BF16), compiler_params=_params(),
    )(uc, ln_w, ln_b)


def _ln_silu_bwd(du3, uc, ln_w, ln_b, name):
    s, ch = uc.shape
    tm = 256

    def body(d_ref, u_ref, w_ref, b_ref, du_ref, dw_ref, db_ref):
        u = u_ref[...]
        mu = jnp.mean(u, axis=-1, keepdims=True)
        xc = u - mu
        rstd = lax.rsqrt(jnp.mean(xc * xc, axis=-1, keepdims=True) + EPS)
        xhat = xc * rstd
        z = xhat * w_ref[...] + b_ref[...]
        sg = jax.nn.sigmoid(z)
        dz = d_ref[...] * (sg * (1.0 + z * (1.0 - sg)))
        dxh = dz * w_ref[...]
        du_ref[...] = rstd * (dxh - jnp.mean(dxh, axis=-1, keepdims=True)
                              - xhat * jnp.mean(dxh * xhat, axis=-1, keepdims=True))
        pw = jnp.sum(dz * xhat, axis=0, keepdims=True)
        pb = jnp.sum(dz, axis=0, keepdims=True)
        first = pl.program_id(0) == 0

        @pl.when(first)
        def _():
            dw_ref[...] = pw
            db_ref[...] = pb

        @pl.when(jnp.logical_not(first))
        def _():
            dw_ref[...] += pw
            db_ref[...] += pb

    row = pl.BlockSpec((tm, ch), lambda i: (i, 0))
    vec = pl.BlockSpec((1, ch), lambda i: (0, 0))
    return pl.pallas_call(
        body, name=name, grid=(s // tm,), in_specs=[row, row, vec, vec], out_specs=[row, vec, vec],
        out_shape=[jax.ShapeDtypeStruct((s, ch), F32), jax.ShapeDtypeStruct((1, ch), F32),
                   jax.ShapeDtypeStruct((1, ch), F32)],
        compiler_params=_params(),
    )(du3, uc, ln_w, ln_b)


def _conv_bwd(duc, u0, proj, conv_w, col0, name, rider=None):
    s = proj.shape[0]
    ch = conv_w.shape[1]
    nblk = ch // LANES
    a0 = col0 // LANES
    tr = 256
    half = (CONV_WIDTH - 1) // 2
    shift = CONV_PAD - half

    def body(duc_ref, u0_ref, a_ref, b_ref, w_ref, da_ref, db_ref, dw_ref, dbias_ref, pad_d, pad_u):
        z = jnp.zeros((CONV_PAD, LANES), F32)
        for buf in (pad_d, pad_u):
            buf[0:CONV_PAD, :] = z
            buf[s + CONV_PAD: s + 2 * CONV_PAD, :] = z

        def fill(rows):
            dst = pl.ds(rows.start + CONV_PAD, rows.size)
            pad_d[dst, :] = duc_ref[rows, :]
            pad_u[dst, :] = u0_ref[rows, :]

        _row_chunks(s, fill)
        dw_acc = [jnp.zeros((8, LANES), F32) for _ in range(CONV_WIDTH)]
        dbias_acc = jnp.zeros((8, LANES), F32)
        for t in range(0, s, tr):
            d_t = duc_ref[t:t + tr, :]
            dbias_acc = dbias_acc + jnp.sum(d_t.reshape(tr // 8, 8, LANES), axis=0)
            du0 = jnp.zeros((tr, LANES), F32)
            for k in range(CONV_WIDTH):
                du0 = du0 + w_ref[k:k + 1, :] * pad_d[t - k + half + CONV_PAD: t - k + half + CONV_PAD + tr, :]
                prod = d_t * pad_u[t + k + shift: t + k + shift + tr, :]
                dw_acc[k] = dw_acc[k] + jnp.sum(prod.reshape(tr // 8, 8, LANES), axis=0)
            av = a_ref[t:t + tr, :]
            sg = jax.nn.sigmoid(b_ref[t:t + tr, :])
            da_ref[t:t + tr, :] = (du0 * sg).astype(BF16)
            db_ref[t:t + tr, :] = (du0 * av * sg * (1.0 - sg)).astype(BF16)
        for k in range(CONV_WIDTH):
            dw_ref[k:k + 1, :] = jnp.sum(dw_acc[k], axis=0, keepdims=True)
        dbias_ref[...] = jnp.sum(dbias_acc, axis=0, keepdims=True)

    col = lambda off: pl.BlockSpec((s, LANES), lambda c: (0, off + c))
    return _pallas(
        body, name=name, grid=(nblk,),
        in_specs=[col(0), col(0), col(a0), col(a0 + nblk),
                  pl.BlockSpec((CONV_WIDTH, LANES), lambda c: (0, c))],
        out_specs=[col(0), col(0), pl.BlockSpec((CONV_WIDTH, LANES), lambda c: (0, c)),
                   pl.BlockSpec((1, LANES), lambda c: (0, c))],
        out_shape=[jax.ShapeDtypeStruct((s, ch), BF16)] * 2
        + [jax.ShapeDtypeStruct((CONV_WIDTH, ch), F32), jax.ShapeDtypeStruct((1, ch), F32)],
        operands=[duc, u0, proj, proj, conv_w],
        scratch_shapes=[pltpu.VMEM((s + 2 * CONV_PAD, LANES), F32)] * 2, rider=rider)


GATE_BLK = 512


def _gate_fwd(proj, bg, y_a, y_b, col0, name):
    s, d = y_a.shape
    tm = 256
    g0 = col0 // GATE_BLK
    nb = d // GATE_BLK

    def body(ga_ref, gb_ref, ba_ref, bb_ref, ya_ref, yb_ref, o_ref):
        g_a = jax.nn.sigmoid(ga_ref[...] + ba_ref[...])
        g_b = jax.nn.sigmoid(gb_ref[...] + bb_ref[...])
        o_ref[...] = (g_a * ya_ref[...] + g_b * yb_ref[...]).astype(BF16)

    act = pl.BlockSpec((tm, GATE_BLK), lambda i, j: (i, j))
    return pl.pallas_call(
        body, name=name, grid=(s // tm, nb),
        in_specs=[pl.BlockSpec((tm, GATE_BLK), lambda i, j: (i, g0 + j)),
                  pl.BlockSpec((tm, GATE_BLK), lambda i, j: (i, g0 + nb + j)),
                  pl.BlockSpec((None, 1, GATE_BLK), lambda i, j: (0, 0, j)),
                  pl.BlockSpec((None, 1, GATE_BLK), lambda i, j: (1, 0, j)), act, act],
        out_specs=act, out_shape=jax.ShapeDtypeStruct((s, d), BF16), compiler_params=_params(),
    )(proj, proj, bg, bg, y_a, y_b)


def _gate_bwd(d_mixed, proj, bg, y_a, y_b, col0, name, rider=None):
    s, d = y_a.shape
    tm = 256
    half = d // 2
    assert col0 % half == 0
    c0 = col0 // half

    def body(dm_ref, a0_ref, a1_ref, b0_ref, b1_ref, bias_ref, ya_ref, yb_ref, dgl_ref, dya_ref, dyb_ref, db_ref):
        dm = dm_ref[...]
        parts = []
        for br, (lo_ref, hi_ref, y_ref, dy_ref) in enumerate(((a0_ref, a1_ref, ya_ref, dya_ref),
                                                              (b0_ref, b1_ref, yb_ref, dyb_ref))):
            logits = jnp.concatenate([lo_ref[...], hi_ref[...]], axis=1)
            gate = jax.nn.sigmoid(logits + bias_ref[br])
            dy_ref[...] = (dm * gate).astype(BF16)
            dgl = dm * y_ref[...] * gate * (1.0 - gate)
            dgl_ref[:, br * d:(br + 1) * d] = dgl.astype(BF16)
            parts.append(jnp.sum(dgl, axis=0, keepdims=True))
        part = jnp.concatenate(parts, axis=0)
        first = pl.program_id(0) == 0

        @pl.when(first)
        def _():
            db_ref[...] = part

        @pl.when(jnp.logical_not(first))
        def _():
            db_ref[...] += part

    row = pl.BlockSpec((tm, d), lambda i: (i, 0))
    logit_blk = lambda k: pl.BlockSpec((tm, half), functools.partial(lambda i, k: (i, c0 + k), k=k))
    return _pallas(
        body, name=name, grid=(s // tm,),
        in_specs=[row, logit_blk(0), logit_blk(1), logit_blk(2), logit_blk(3),
                  pl.BlockSpec((2, 1, d), lambda i: (0, 0, 0)), row, row],
        out_specs=[pl.BlockSpec((tm, 2 * d), lambda i: (i, 0)), row, row, pl.BlockSpec((2, d), lambda i: (0, 0))],
        out_shape=[jax.ShapeDtypeStruct((s, 2 * d), BF16), jax.ShapeDtypeStruct((s, d), BF16),
                   jax.ShapeDtypeStruct((s, d), BF16), jax.ShapeDtypeStruct((2, d), F32)],
        operands=[d_mixed, proj, proj, proj, proj, bg, y_a, y_b], rider=rider)


def _ffn_in_swiglu(h2, w_blocked, name):
    s, k = h2.shape
    nblk, _, tn = w_blocked.shape
    ff = nblk // 2 * tn
    tm = 512

    def body(a_ref, wg_ref, wu_ref, g_ref, u_ref, act_ref):
        a = a_ref[...]
        gt = jnp.dot(a, wg_ref[...], preferred_element_type=F32)
        up = jnp.dot(a, wu_ref[...], preferred_element_type=F32)
        g_ref[...] = gt
        u_ref[...] = up
        act_ref[...] = (gt * jax.nn.sigmoid(gt) * up).astype(BF16)

    out = pl.BlockSpec((tm, tn), lambda i, j: (i, j))
    return pl.pallas_call(
        body, name=name, grid=(s // tm, nblk // 2),
        in_specs=[pl.BlockSpec((tm, k), lambda i, j: (i, 0)),
                  pl.BlockSpec((None, k, tn), lambda i, j: (j, 0, 0)),
                  pl.BlockSpec((None, k, tn), lambda i, j: (nblk // 2 + j, 0, 0))],
        out_specs=[out, out, out],
        out_shape=[jax.ShapeDtypeStruct((s, ff), F32), jax.ShapeDtypeStruct((s, ff), F32),
                   jax.ShapeDtypeStruct((s, ff), BF16)],
        compiler_params=_params(),
    )(h2, w_blocked, w_blocked)


def _swiglu_bwd(gate, up, d_act, name, rider=None):
    s, ff = gate.shape
    tm = 256

    def body(g_ref, u_ref, d_ref, o_ref):
        gt = g_ref[...]
        sg = jax.nn.sigmoid(gt)
        dv = d_ref[...]
        o_ref[:, 0:ff] = (dv * u_ref[...] * (sg * (1.0 + gt * (1.0 - sg)))).astype(BF16)
        o_ref[:, ff:2 * ff] = (dv * gt * sg).astype(BF16)

    row = pl.BlockSpec((tm, ff), lambda i: (i, 0))
    return _pallas(
        body, name=name, grid=(s // tm,), in_specs=[row, row, row],
        out_specs=pl.BlockSpec((tm, 2 * ff), lambda i: (i, 0)),
        out_shape=jax.ShapeDtypeStruct((s, 2 * ff), BF16), operands=[gate, up, d_act], rider=rider)


def _out_proj_rmsnorm(mixed, w_out, x, norm_w, name):
    s, k = mixed.shape
    d = w_out.shape[1]
    tm = 512

    def body(a_ref, w_ref, x_ref, nw_ref, x1_ref, h2_ref):
        x1 = x_ref[...] + jnp.dot(a_ref[...], w_ref[...], preferred_element_type=F32)
        x1_ref[...] = x1
        rstd = lax.rsqrt(jnp.mean(x1 * x1, axis=-1, keepdims=True) + EPS)
        h2_ref[...] = (x1 * rstd * nw_ref[...]).astype(BF16)

    row = pl.BlockSpec((tm, d), lambda i: (i, 0))
    return pl.pallas_call(
        body, name=name, grid=(s // tm,),
        in_specs=[pl.BlockSpec((tm, k), lambda i: (i, 0)), pl.BlockSpec((k, d), lambda i: (0, 0)), row,
                  pl.BlockSpec((1, d), lambda i: (0, 0))],
        out_specs=[row, row],
        out_shape=[jax.ShapeDtypeStruct((s, d), F32), jax.ShapeDtypeStruct((s, d), BF16)],
        compiler_params=_params(),
    )(mixed, w_out, x, norm_w)


def _ffn_out_loss(act, w_ffn_out, x1, target, name):
    s, k = act.shape
    d = w_ffn_out.shape[1]
    tm = 512

    def body(a_ref, w_ref, x1_ref, t_ref, dy_ref, dyb_ref, loss_ref, acc):
        y = x1_ref[...] + jnp.dot(a_ref[...], w_ref[...], preferred_element_type=F32)
        diff = y - t_ref[...]
        dy = diff * (1.0 / d)
        dy_ref[...] = dy
        dyb_ref[...] = dy.astype(BF16)
        part = jnp.sum((diff * diff).reshape(tm // 8, 8, d), axis=0)
        i = pl.program_id(0)

        @pl.when(i == 0)
        def _():
            acc[...] = part

        @pl.when(i > 0)
        def _():
            acc[...] += part

        @pl.when(i == pl.num_programs(0) - 1)
        def _():
            loss_ref[...] = (0.5 / d) * jnp.sum(jnp.sum(acc[...], axis=1, keepdims=True), axis=0, keepdims=True)

    row = pl.BlockSpec((tm, d), lambda i: (i, 0))
    return pl.pallas_call(
        body, name=name, grid=(s // tm,),
        in_specs=[pl.BlockSpec((tm, k), lambda i: (i, 0)), pl.BlockSpec((k, d), lambda i: (0, 0)), row, row],
        out_specs=[row, row, pl.BlockSpec((1, 1), lambda i: (0, 0))],
        out_shape=[jax.ShapeDtypeStruct((s, d), F32), jax.ShapeDtypeStruct((s, d), BF16),
                   jax.ShapeDtypeStruct((1, 1), F32)],
        scratch_shapes=[pltpu.VMEM((8, d), F32)], compiler_params=_params(),
    )(act, w_ffn_out, x1, target)


LATE_GATHER = ("w_o_attn", "w_pw_conv", "w_out", "w_ffn_in", "w_ffn_out")
EARLY_REDUCE = LATE_GATHER


def _blocks_by_half(g):
    if g.ndim == 2:
        g = g.reshape(N_CHIPS, g.shape[0] // N_CHIPS, g.shape[1])
    return g.reshape(N_CHIPS, 2, g.shape[1] // 2, g.shape[2])


def _forward_backward(x, pos_col, target, wts, late_bufs, pos_arr):
    wts = dict(wts)
    consts = _rope_consts()
    bd = consts[3]
    qw2 = jnp.tile(wts["q_norm_w"], (1, LANES // HEAD_DIM))
    kw2 = jnp.tile(wts["k_norm_w"], (1, LANES // HEAD_DIM))
    qkv_w = 3 * N_SLOT_HEADS * HEAD_DIM
    conv_col0 = 3 * qkv_w
    ch = wts["conv_w"].shape[1]
    gate_col0 = conv_col0 + 2 * ch

    h = _rmsnorm_fwd(x, wts["norm1_w"], "rms1_fwd")
    gathering, started = _split_start(_gather_ici_rider(late_bufs, []), "late_gather_start", after=[wts["w_in"]])
    proj = _matmul(h, wts["w_in"], mode="nn", tm=512, tn=1920, tk=1024, out_dtype=F32, name="mm_proj",
                   b_blocked=True, after=[started])
    qn, kn = _qk_fwd(proj, pos_col, qw2, kw2, consts, "qk_fwd")
    attn, lse, attn_b = _attn_fwd(qn, kn, proj, "attn_fwd")
    late_bufs = _split_wait(gathering, after=[attn_b], name="late_gather_wait")
    (u0, uc), late_bufs = _conv_fwd(proj, wts["conv_w"], wts["conv_b"], conv_col0, "conv_fwd",
                                    rider=_gather_forward_rider(late_bufs))
    for n, buf in zip(LATE_GATHER, late_bufs):
        full = buf.reshape(N_CHIPS, -1, buf.shape[3])
        wts[n] = full.reshape(-1, full.shape[2]) if n in ROW_SHARDED else full
    y_a = _matmul(attn_b, wts["w_o_attn"], mode="nn", tm=1024, tn=256, tk=512, out_dtype=F32, name="mm_ya",
                  b_blocked=True)
    u3 = _ln_silu_fwd(uc, wts["conv_ln_w"], wts["conv_ln_b"], "ln_fwd")
    y_b = _matmul(u3, wts["w_pw_conv"], mode="nn", tm=1024, tn=256, tk=512, out_dtype=F32, name="mm_yb",
                  b_blocked=True)
    mixed = _gate_fwd(proj, wts["b_gate"], y_a, y_b, gate_col0, "gate_fwd")
    x1, h2 = _out_proj_rmsnorm(mixed, wts["w_out"], x, wts["norm2_w"], "mm_x1_rms2")
    gate, up, act = _ffn_in_swiglu(h2, wts["w_ffn_in"], "mm_gu_swiglu")
    dy, dy_b16, loss = _ffn_out_loss(act, wts["w_ffn_out"], x1, target, "mm_x2_loss")

    g = {}
    by_chip = {}

    def pair_add(n, blocks, received):
        return _add_own_half(blocks, received, pos_arr, f"grads_pair_add_{n}")

    d_act = _matmul(dy_b16, wts["w_ffn_out"], mode="nt", tm=512, tn=1408, tk=1024, out_dtype=F32, name="mm_dact")
    g_ffn_out = _blocks_by_half(
        _matmul(act, dy_b16, mode="tn", tm=1408, tn=1024, tk=2048, out_dtype=F32, name="mm_dwffnout"))
    dgu, (received,) = _swiglu_bwd(gate, up, d_act, "swiglu_bwd",
                                   rider=_pair_exchange_rider([g_ffn_out], halved=True))
    to_send, own = pair_add("w_ffn_out", g_ffn_out, received)
    dh2, (by_chip["w_ffn_out"],) = _matmul(
        dgu, wts["w_ffn_in"], mode="nt", tm=1024, tn=1024, tk=1408, out_dtype=F32, name="mm_dh2", b_blocked=True,
        rider=_chip_exchange_rider([to_send], [own]))
    g_ffn_in = _blocks_by_half(_matmul(h2, dgu, mode="tn", tm=512, tn=1408, tk=2048, out_dtype=F32,
                                       name="mm_dwffnin", out_blocked=N_CHIPS))
    exchanging, started = _split_start(_pair_exchange_rider([g_ffn_in], halved=True), "grads_ffn_in_pair_start")
    dx1, dx1_b16, g["norm2_w"] = _rmsnorm_bwd(dh2, x1, wts["norm2_w"], dy, "rms2_bwd")
    d_mixed = _matmul(dx1_b16, wts["w_out"], mode="nt", tm=512, tn=1024, tk=1024, out_dtype=F32, name="mm_dmixed",
                      after=[started])
    g["w_out"] = _matmul(mixed, dx1_b16, mode="tn", tm=512, tn=1024, tk=2048, out_dtype=F32, name="mm_dwout")
    dgl, dy_a, dy_b, g["b_gate"] = _gate_bwd(d_mixed, proj, wts["b_gate"], y_a, y_b, gate_col0, "gate_bwd")
    (received,) = _split_wait(exchanging, after=[dgl], name="grads_ffn_in_pair_wait")
    ffn_in_to_send, ffn_in_own = pair_add("w_ffn_in", g_ffn_in, received)
    dattn = _matmul(dy_a, wts["w_o_attn"], mode="nt", tm=1024, tn=512, tk=256, out_dtype=F32, name="mm_dattn",
                    b_blocked=True)
    g["w_o_attn"] = _matmul(attn_b, dy_a, mode="tn", tm=512, tn=256, tk=2048, out_dtype=F32, name="mm_dwo",
                            out_blocked=N_CHIPS)
    du3 = _matmul(dy_b, wts["w_pw_conv"], mode="nt", tm=1024, tn=512, tk=256, out_dtype=F32, name="mm_du3",
                  b_blocked=True)
    g["w_pw_conv"] = _matmul(u3, dy_b, mode="tn", tm=512, tn=256, tk=2048, out_dtype=F32, name="mm_dwpw",
                             out_blocked=N_CHIPS)
    duc, g["conv_ln_w"], g["conv_ln_b"] = _ln_silu_bwd(du3, uc, wts["conv_ln_w"], wts["conv_ln_b"], "ln_bwd")

    small3 = ("w_out", "w_o_attn", "w_pw_conv")
    g_small3 = [_blocks_by_half(g.pop(n)) for n in small3]
    (da, db, g["conv_w"], g["conv_b"]), received = _conv_bwd(
        duc, u0, proj, wts["conv_w"], conv_col0, "conv_bwd", rider=_pair_exchange_rider(g_small3, halved=True))
    sums3 = [pair_add(n, gb, rv) for n, gb, rv in zip(small3, g_small3, received)]
    (dqn, dkn, dv), (by_chip["w_ffn_in"],) = _attn_bwd(
        qn, kn, proj, dattn, attn, lse, bd, "attn_bwd",
        rider=_chip_exchange_rider([ffn_in_to_send], [ffn_in_own]))
    (dproj, dqw, dkw), exchanged3 = _qk_bwd(
        dqn, dkn, dv, da, db, dgl, proj, pos_col, qw2, kw2, consts, "qk_bwd",
        rider=_chip_exchange_rider([s[0] for s in sums3], [s[1] for s in sums3]))
    by_chip.update(zip(small3, exchanged3))
    halves = [_sum_chips(by_chip[n], pos_arr, f"grads_chip_sum_{n}") for n in EARLY_REDUCE]
    g["q_norm_w"] = dqw[:, :HEAD_DIM]
    g["k_norm_w"] = dkw[:, :HEAD_DIM]

    c = pos_arr[0]
    rh = h.shape[1] // 2
    h_sibling = lax.dynamic_slice_in_dim(h, (1 - c) * rh, rh, axis=1)
    h_own = lax.dynamic_slice_in_dim(h, c * rh, rh, axis=1)
    g_sibling, shards = _matmul(h_sibling, dproj, mode="tn", tm=rh, tn=1920, tk=2048, out_dtype=F32,
                                name="mm_dwin_sibling", out_blocked=N_CHIPS, rider=_pair_gather_rider(halves))
    reduced = dict(zip(EARLY_REDUCE, shards))
    exchanging, started = _split_start(_pair_exchange_rider([g_sibling], halved=False), "grads_w_in_pair_start")
    g_own = _matmul(h_own, dproj, mode="tn", tm=rh, tn=1920, tk=2048, out_dtype=F32, name="mm_dwin_own",
                    out_blocked=N_CHIPS, after=[started])
    (from_sibling,) = _split_wait(exchanging, after=[g_own], name="grads_w_in_pair_wait")
    to_send, own = _add_own_half(g_own, from_sibling, pos_arr, "grads_pair_add_w_in")
    in_flight, started = _split_start(_chip_exchange_rider([to_send], [own]), "grads_w_in_exchange_start")
    dh = _matmul(dproj, wts["w_in"], mode="nt", tm=1024, tn=1024, tk=1920, out_dtype=F32, name="mm_dh",
                 b_blocked=True, after=[started])
    grad_x, _, g["norm1_w"] = _rmsnorm_bwd(dh, x, wts["norm1_w"], dx1, "rms1_bwd")
    return loss, grad_x, g, reduced, (in_flight, started)


def _mesh_pos():
    return lax.axis_index("x"), lax.axis_index("y"), lax.axis_index("c")


def _other_chips(x, y):
    return [(1 - x, y), (x, 1 - y), (1 - x, 1 - y)]


def _cast_into_slot(shard, chip_arr, dtype, name):
    r, c = shard.shape
    tr = r // 2 if r % 32 == 0 else r

    def body(chip_ref, s_ref, o_ref):
        del chip_ref
        o_ref[...] = s_ref[...].astype(dtype)

    return pl.pallas_call(
        body, name=name,
        grid_spec=pltpu.PrefetchScalarGridSpec(
            num_scalar_prefetch=1, grid=(r // tr,),
            in_specs=[pl.BlockSpec((tr, c), lambda i, chip_ref: (i, 0))],
            out_specs=pl.BlockSpec((None, tr, c), lambda i, chip_ref: (chip_ref[0], i, 0))),
        out_shape=jax.ShapeDtypeStruct((N_CHIPS, r, c), dtype), compiler_params=_params(),
    )(chip_arr, shard)


def _gather_both_legs_rider(big, small):
    nb = len(big)
    n = nb + len(small)

    def part(bufs, a, slot, half):
        return bufs[a].at[slot, half] if a < nb else bufs[a].at[slot]

    def start(r_in, bufs, sems):
        x, y, c = _mesh_pos()
        me = 2 * x + y
        chips = _other_chips(x, y)
        for a in range(n):
            for k in range(3):
                px, py = chips[k]
                pltpu.make_async_remote_copy(
                    src_ref=part(bufs, a, me, c), dst_ref=part(bufs, a, me, c), send_sem=sems[0].at[a, k],
                    recv_sem=sems[1].at[a, k], device_id=(px, py, c), device_id_type=MESH).start()

    def wait(r_in, bufs, sems):
        x, y, c = _mesh_pos()
        me = 2 * x + y
        chips = _other_chips(x, y)
        sibling = (x, y, 1 - c)
        forwards = []
        for a in range(n):
            for k in range(3):
                px, py = chips[k]
                slot = 2 * px + py
                pltpu.make_async_remote_copy(
                    src_ref=part(bufs, a, slot, c), dst_ref=part(bufs, a, slot, c), send_sem=sems[0].at[a, k],
                    recv_sem=sems[1].at[a, k], device_id=(px, py, c), device_id_type=MESH).wait_recv()
                if a < nb:
                    fwd = pltpu.make_async_remote_copy(
                        src_ref=part(bufs, a, slot, c), dst_ref=part(bufs, a, slot, c), send_sem=sems[2].at[a, k],
                        recv_sem=sems[3].at[a, k], device_id=sibling, device_id_type=MESH)
                    fwd.start()
                    forwards.append(fwd)
        for a in range(nb):
            for k in range(3):
                px, py = chips[k]
                slot = 2 * px + py
                pltpu.make_async_remote_copy(
                    src_ref=part(bufs, a, slot, 1 - c), dst_ref=part(bufs, a, slot, 1 - c),
                    send_sem=sems[2].at[a, k], recv_sem=sems[3].at[a, k], device_id=sibling,
                    device_id_type=MESH).wait_recv()
        for a in range(n):
            for k in range(3):
                px, py = chips[k]
                pltpu.make_async_remote_copy(
                    src_ref=part(bufs, a, me, c), dst_ref=part(bufs, a, me, c), send_sem=sems[0].at[a, k],
                    recv_sem=sems[1].at[a, k], device_id=(px, py, c), device_id_type=MESH).wait_send()
        for fwd in forwards:
            fwd.wait_send()

    ops = list(big) + list(small)
    return _Rider(ops, [jax.ShapeDtypeStruct(o.shape, o.dtype) for o in ops], {i: i for i in range(n)},
                  [pltpu.SemaphoreType.DMA((n, 3)), pltpu.SemaphoreType.DMA((n, 3)),
                   pltpu.SemaphoreType.DMA((nb, 3)), pltpu.SemaphoreType.DMA((nb, 3))], start, wait)


def _comm_call(rider, name):
    def body():
        pass

    return _pallas(body, name=name, grid=(1,), in_specs=[], out_specs=[], out_shape=[], operands=[],
                   rider=rider)[1]


def _gather_ici_rider(big, small):
    nb = len(big)
    n = nb + len(small)

    def copies(bufs, sems):
        x, y, c = _mesh_pos()
        me = 2 * x + y
        part = lambda a, slot: bufs[a].at[slot, c] if a < nb else bufs[a].at[slot]
        out = []
        for a in range(n):
            for k, (px, py) in enumerate(_other_chips(x, y)):
                send = functools.partial(
                    pltpu.make_async_remote_copy,
                    src_ref=part(a, me), dst_ref=part(a, me), send_sem=sems[0].at[a, k],
                    recv_sem=sems[1].at[a, k], device_id=(px, py, c), device_id_type=MESH)
                recv = functools.partial(
                    pltpu.make_async_remote_copy,
                    src_ref=part(a, 2 * px + py), dst_ref=part(a, 2 * px + py), send_sem=sems[0].at[a, k],
                    recv_sem=sems[1].at[a, k], device_id=(px, py, c), device_id_type=MESH)
                out.append((send, recv))
        return out

    def start(r_in, r_out, sems):
        for send, _ in copies(r_out, sems):
            send().start()

    def wait(r_in, r_out, sems):
        cps = copies(r_out, sems)
        for _, recv in cps:
            recv().wait_recv()
        for send, _ in cps:
            send().wait_send()

    ops = list(big) + list(small)
    return _Rider(ops, [jax.ShapeDtypeStruct(o.shape, o.dtype) for o in ops], {i: i for i in range(n)},
                  [pltpu.SemaphoreType.DMA((n, 3)), pltpu.SemaphoreType.DMA((n, 3))], start, wait)


def _gather_forward_rider(big):
    n = len(big)

    def copies(bufs, sems):
        x, y, c = _mesh_pos()
        out = []
        for a in range(n):
            for k, (px, py) in enumerate(_other_chips(x, y)):
                slot = 2 * px + py
                send = functools.partial(
                    pltpu.make_async_remote_copy,
                    src_ref=bufs[a].at[slot, c], dst_ref=bufs[a].at[slot, c], send_sem=sems[0].at[a, k],
                    recv_sem=sems[1].at[a, k], device_id=(x, y, 1 - c), device_id_type=MESH)
                recv = functools.partial(
                    pltpu.make_async_remote_copy,
                    src_ref=bufs[a].at[slot, 1 - c], dst_ref=bufs[a].at[slot, 1 - c], send_sem=sems[0].at[a, k],
                    recv_sem=sems[1].at[a, k], device_id=(x, y, 1 - c), device_id_type=MESH)
                out.append((send, recv))
        return out

    def start(r_in, r_out, sems):
        for send, _ in copies(r_out, sems):
            send().start()

    def wait(r_in, r_out, sems):
        cps = copies(r_out, sems)
        for _, recv in cps:
            recv().wait_recv()
        for send, _ in cps:
            send().wait_send()

    return _Rider(big, [jax.ShapeDtypeStruct(o.shape, o.dtype) for o in big], {i: i for i in range(n)},
                  [pltpu.SemaphoreType.DMA((n, 3)), pltpu.SemaphoreType.DMA((n, 3))], start, wait)


def _pair_exchange_rider(gs, halved):
    n = len(gs)

    def copies(r_in, r_out, sems):
        x, y, c = _mesh_pos()
        return [pltpu.make_async_remote_copy(
            src_ref=r_in[a].at[:, 1 - c] if halved else r_in[a], dst_ref=r_out[a], send_sem=sems[0].at[a],
            recv_sem=sems[1].at[a], device_id=(x, y, 1 - c), device_id_type=MESH) for a in range(n)]

    def start(r_in, r_out, sems):
        for cp in copies(r_in, r_out, sems):
            cp.start()

    def wait(r_in, r_out, sems):
        for cp in copies(r_in, r_out, sems):
            cp.wait()

    return _Rider(gs, [jax.ShapeDtypeStruct((g.shape[0],) + g.shape[-2:], g.dtype) for g in gs], {},
                  [pltpu.SemaphoreType.DMA((n,)), pltpu.SemaphoreType.DMA((n,))], start, wait)


def _chip_exchange_rider(to_send, by_chip, row_range=None):
    n = len(to_send)

    def copies(r_in, r_out, sems):
        x, y, c = _mesh_pos()
        me = 2 * x + y
        rows = (lambda ref: ref) if row_range is None else (lambda ref: ref.at[pl.ds(*row_range)])
        out = []
        for a in range(n):
            for k, (px, py) in enumerate(_other_chips(x, y)):
                send = functools.partial(
                    pltpu.make_async_remote_copy,
                    src_ref=rows(r_in[a].at[2 * px + py]), dst_ref=rows(r_out[a].at[me]),
                    send_sem=sems[0].at[a, k], recv_sem=sems[1].at[a, k], device_id=(px, py, c),
                    device_id_type=MESH)
                recv = functools.partial(
                    pltpu.make_async_remote_copy,
                    src_ref=rows(r_in[a].at[me]), dst_ref=rows(r_out[a].at[2 * px + py]),
                    send_sem=sems[0].at[a, k], recv_sem=sems[1].at[a, k], device_id=(px, py, c),
                    device_id_type=MESH)
                out.append((send, recv))
        return out

    def start(r_in, r_out, sems):
        for send, _ in copies(r_in, r_out, sems):
            send().start()

    def wait(r_in, r_out, sems):
        cps = copies(r_in, r_out, sems)
        for _, recv in cps:
            recv().wait_recv()
        for send, _ in cps:
            send().wait_send()

    return _Rider(list(to_send) + list(by_chip), [jax.ShapeDtypeStruct(b.shape, b.dtype) for b in by_chip],
                  {n + i: i for i in range(n)},
                  [pltpu.SemaphoreType.DMA((n, 3)), pltpu.SemaphoreType.DMA((n, 3))], start, wait)


HBM = pl.BlockSpec(memory_space=pltpu.HBM)
SEM = pl.BlockSpec(memory_space=pltpu.SEMAPHORE)


_IN_FLIGHT = pltpu.CompilerParams(has_side_effects=pltpu.SideEffectType.DATAFLOW_SIDE_EFFECTING)


class _FlatSems:
    def __init__(self, ref, shape):
        self.ref, self.shape = ref, shape

    @property
    def at(self):
        return self

    def __getitem__(self, idx):
        idx = idx if isinstance(idx, tuple) else (idx,)
        flat = 0
        for i, n in zip(idx, self.shape):
            flat = flat * n + i
        return self.ref.at[flat]


def _flat_sem_types(rider):
    return tuple(pltpu.SemaphoreType.DMA((int(np.prod(s.shape)),)) for s in rider.scratch)


def _as_rider_sems(rider, refs):
    return [_FlatSems(r, s.shape) for r, s in zip(refs, rider.scratch)]


def _split_start(rider, name, after=()):
    n_in, n_out, n_sem = len(rider.operands), len(rider.out_shapes), len(rider.scratch)
    n_after = len(after)
    fresh = [j for j in range(n_out) if j not in rider.aliases.values()]
    by_out = {j: i for i, j in rider.aliases.items()}

    def body(*refs):
        r_in = refs[:n_in]
        refs = refs[n_in + n_after:]
        sems = refs[:n_sem]
        thru = refs[n_sem:n_sem + n_in]
        fresh_refs = refs[n_sem + n_in:n_sem + n_in + len(fresh)]
        token = refs[-1]
        r_out = [thru[by_out[j]] if j in by_out else fresh_refs[fresh.index(j)] for j in range(n_out)]
        rider.start(r_in, r_out, _as_rider_sems(rider, sems))
        token[...] = jnp.zeros_like(token)

    res = pl.pallas_call(
        body, name=name,
        out_shape=_flat_sem_types(rider) + tuple(pltpu.HBM(o.shape, o.dtype) for o in rider.operands)
        + tuple(pltpu.HBM(rider.out_shapes[j].shape, rider.out_shapes[j].dtype) for j in fresh)
        + (jax.ShapeDtypeStruct((8, LANES), F32),),
        in_specs=(HBM,) * n_in + (ANY,) * n_after,
        out_specs=(SEM,) * n_sem + (HBM,) * (n_in + len(fresh)) + (pl.BlockSpec(memory_space=pltpu.VMEM),),
        input_output_aliases={i: n_sem + i for i in range(n_in)}, compiler_params=_IN_FLIGHT,
    )(*[pltpu.with_memory_space_constraint(o, pltpu.HBM) for o in rider.operands], *after)
    return (rider, res[:n_sem], res[n_sem:n_sem + n_in], res[n_sem + n_in:-1]), res[-1]


def _split_wait(handles, after, name):
    rider, sems, thru, fresh_arrays = handles
    n_in, n_out, n_sem = len(rider.operands), len(rider.out_shapes), len(rider.scratch)
    fresh = [j for j in range(n_out) if j not in rider.aliases.values()]
    by_out = {j: i for i, j in rider.aliases.items()}
    n_data = n_in + len(fresh)

    def body(*refs):
        r_in = refs[:n_in]
        fresh_refs = refs[n_in:n_data]
        sem_refs = refs[n_data:n_data + n_sem]
        r_out = [r_in[by_out[j]] if j in by_out else fresh_refs[fresh.index(j)] for j in range(n_out)]
        rider.wait(r_in, r_out, _as_rider_sems(rider, sem_refs))

    data = list(thru) + list(fresh_arrays)
    res = pl.pallas_call(
        body, name=name, out_shape=tuple(pltpu.HBM(d.shape, d.dtype) for d in data),
        in_specs=(HBM,) * n_data + (SEM,) * n_sem + (ANY,) * len(after), out_specs=(HBM,) * n_data,
        input_output_aliases={i: i for i in range(n_data)}, compiler_params=_IN_FLIGHT,
    )(*data, *sems, *after)
    return [res[by_out[j]] if j in by_out else res[n_in + fresh.index(j)] for j in range(n_out)]


def _pair_gather_rider(bufs):
    n = len(bufs)

    def copies(r_out, sems):
        x, y, c = _mesh_pos()
        out = []
        for a in range(n):
            send = functools.partial(
                    pltpu.make_async_remote_copy,
                src_ref=r_out[a].at[c], dst_ref=r_out[a].at[c], send_sem=sems[0].at[a],
                recv_sem=sems[1].at[a], device_id=(x, y, 1 - c), device_id_type=MESH)
            recv = functools.partial(
                    pltpu.make_async_remote_copy,
                src_ref=r_out[a].at[1 - c], dst_ref=r_out[a].at[1 - c], send_sem=sems[0].at[a],
                recv_sem=sems[1].at[a], device_id=(x, y, 1 - c), device_id_type=MESH)
            out.append((send, recv))
        return out

    def start(r_in, r_out, sems):
        for send, _ in copies(r_out, sems):
            send().start()

    def wait(r_in, r_out, sems):
        cps = copies(r_out, sems)
        for _, recv in cps:
            recv().wait_recv()
        for send, _ in cps:
            send().wait_send()

    return _Rider(bufs, [jax.ShapeDtypeStruct(b.shape, b.dtype) for b in bufs], {i: i for i in range(n)},
                  [pltpu.SemaphoreType.DMA((n,)), pltpu.SemaphoreType.DMA((n,))], start, wait)


def _add_own_half(g, recv, pos_arr, name):
    nb, rh, cols = g.shape[0], g.shape[-2], g.shape[-1]

    def body(pos_ref, g_ref, r_ref, send_ref, own_ref):
        s = (g_ref[...] + r_ref[...]).astype(BF16)
        send_ref[...] = s

        @pl.when(pl.program_id(0) == pos_ref[1])
        def _():
            own_ref[...] = s

    blk = pl.BlockSpec((None, rh, cols), lambda j, pos_ref: (j, 0, 0))
    g_spec = blk if g.ndim == 3 else pl.BlockSpec((None, None, rh, cols),
                                                   lambda j, pos_ref: (j, pos_ref[0], 0, 0))
    shape = jax.ShapeDtypeStruct((nb, rh, cols), BF16)
    return pl.pallas_call(
        body, name=name,
        grid_spec=pltpu.PrefetchScalarGridSpec(
            num_scalar_prefetch=1, grid=(nb,), in_specs=[g_spec, blk],
            out_specs=[blk, pl.BlockSpec((None, rh, cols), lambda j, pos_ref: (pos_ref[1], 0, 0))]),
        out_shape=[shape, shape], compiler_params=_params(),
    )(pos_arr, g, recv)


def _sum_chips(gath, pos_arr, name):
    nb, rh, cols = gath.shape

    def body(pos_ref, a_ref, b_ref, c_ref, d_ref, o_ref):
        del pos_ref
        o_ref[...] = ((a_ref[...].astype(F32) + b_ref[...].astype(F32)) + c_ref[...].astype(F32)) \
            + d_ref[...].astype(F32)

    tr = rh // 2 if (rh // 2) % 16 == 0 else rh
    specs = [pl.BlockSpec((None, tr, cols), functools.partial(lambda i, pos_ref, j: (j, i, 0), j=j))
             for j in range(nb)]
    return pl.pallas_call(
        body, name=name,
        grid_spec=pltpu.PrefetchScalarGridSpec(
            num_scalar_prefetch=1, grid=(rh // tr,), in_specs=specs,
            out_specs=pl.BlockSpec((None, tr, cols), lambda i, pos_ref: (pos_ref[0], i, 0))),
        out_shape=jax.ShapeDtypeStruct((2, rh, cols), F32), compiler_params=_params(),
    )(pos_arr, gath, gath, gath, gath)


def _small_allreduce(v, name, rider=None, after=()):
    n = v.shape[0]
    n_dev = 8

    def body(v_ref, o_ref, buf, send_sems, recv_sems):
        x, y, c = _mesh_pos()
        me = 4 * x + 2 * y + c
        buf[me] = v_ref[...]
        sends = []
        peers = []
        for r in range(1, n_dev):
            px = 1 - x if r & 4 else x
            py = 1 - y if r & 2 else y
            pc = 1 - c if r & 1 else c
            peers.append((px, py, pc))
            cp = pltpu.make_async_remote_copy(
                src_ref=v_ref, dst_ref=buf.at[me], send_sem=send_sems.at[r - 1],
                recv_sem=recv_sems.at[r - 1], device_id=(px, py, pc), device_id_type=MESH)
            cp.start()
            sends.append(cp)
        for r, (px, py, pc) in enumerate(peers):
            pltpu.make_async_remote_copy(
                src_ref=v_ref, dst_ref=buf.at[4 * px + 2 * py + pc], send_sem=send_sems.at[r],
                recv_sem=recv_sems.at[r], device_id=(px, py, pc), device_id_type=MESH).wait_recv()
        for cp in sends:
            cp.wait_send()
        acc = buf[0]
        for i in range(1, n_dev):
            acc = acc + buf[i]
        o_ref[...] = acc

    whole = pl.BlockSpec(v.shape, lambda i: (0, 0))
    return _pallas(
        body, name=name, grid=(1,), in_specs=[whole], out_specs=whole,
        out_shape=jax.ShapeDtypeStruct(v.shape, v.dtype), operands=[v],
        scratch_shapes=[pltpu.VMEM((n_dev, n, LANES), F32), pltpu.SemaphoreType.DMA((n_dev - 1,)),
                        pltpu.SemaphoreType.DMA((n_dev - 1,))],
        rider=rider, after=after)


def _adamw_math(w, g, m, v):
    m = ADAM_B1 * m + (1.0 - ADAM_B1) * g
    v = ADAM_B2 * v + (1.0 - ADAM_B2) * (g * g)
    m_hat = m / (1.0 - ADAM_B1 ** ADAM_STEP)
    v_hat = v / (1.0 - ADAM_B2 ** ADAM_STEP)
    delta = -ADAM_LR * (m_hat / (jnp.sqrt(v_hat) + ADAM_EPS) + ADAM_WD * w)
    return delta, m, v


def _adamw(w, g, m, v, name, after=()):
    r, c = w.shape
    tr = 128 if r % 128 == 0 else 64
    assert r % tr == 0

    def body(w_ref, g_ref, m_ref, v_ref, go_ref, d_ref, mo_ref, vo_ref):
        gv = g_ref[...]
        d, mn, vn = _adamw_math(w_ref[...], gv, m_ref[...], v_ref[...])
        go_ref[...] = gv
        d_ref[...] = d
        mo_ref[...] = mn
        vo_ref[...] = vn

    blk = pl.BlockSpec((tr, c), lambda i: (i, 0))
    return _pallas(body, name=name, grid=(r // tr,), in_specs=[blk] * 4, out_specs=[blk] * 4,
                   out_shape=[jax.ShapeDtypeStruct((r, c), F32)] * 4, operands=[w, g, m, v], after=after)


def _adamw_small(ws, gs, ms, vs, name):
    n = len(ws)

    def body(*refs):
        w_r, g_r, m_r, v_r = refs[:n], refs[n:2 * n], refs[2 * n:3 * n], refs[3 * n:4 * n]
        d_o, m_o, v_o = refs[4 * n:5 * n], refs[5 * n:6 * n], refs[6 * n:7 * n]
        for i in range(n):
            d, mn, vn = _adamw_math(w_r[i][...], g_r[i][...], m_r[i][...], v_r[i][...])
            d_o[i][...] = d
            m_o[i][...] = mn
            v_o[i][...] = vn

    specs = [pl.BlockSpec(w.shape, lambda i: (0, 0)) for w in ws]
    shapes = [jax.ShapeDtypeStruct(w.shape, F32) for w in ws]
    outs = pl.pallas_call(
        body, name=name, grid=(1,), in_specs=specs * 4, out_specs=specs * 3, out_shape=shapes * 3,
        compiler_params=_params(),
    )(*ws, *gs, *ms, *vs)
    return outs[:n], outs[n:2 * n], outs[2 * n:]


BIG = ("w_in", "w_o_attn", "w_pw_conv", "w_out", "w_ffn_in", "w_ffn_out")
ROW_SHARDED = ("w_out", "w_ffn_out")
SMALL = ("norm1_w", "b_gate", "q_norm_w", "k_norm_w", "conv_w", "conv_b", "conv_ln_w", "conv_ln_b", "norm2_w")
ORDER = ("norm1_w", "w_in", "b_gate", "q_norm_w", "k_norm_w", "w_o_attn", "conv_w", "conv_b", "conv_ln_w",
         "conv_ln_b", "w_pw_conv", "w_out", "norm2_w", "w_ffn_in", "w_ffn_out")
PACK_TILE = 8 * LANES


def _pack_small(parts):
    rows = []
    for p in parts:
        flat = p.reshape(-1)
        pad = (-flat.shape[0]) % PACK_TILE
        rows.append(jnp.pad(flat, (0, pad)).reshape(-1, LANES))
    return jnp.concatenate(rows, axis=0)


def _unpack_small(packed, shapes):
    out, row = [], 0
    for shp in shapes:
        size = int(np.prod(shp))
        nrow = -(-size // PACK_TILE) * (PACK_TILE // LANES)
        out.append(packed[row:row + nrow].reshape(-1)[:size].reshape(shp))
        row += nrow
    return out


def kernel(x, positions, norm1_w, w_in, b_gate, q_norm_w, k_norm_w, w_o_attn, conv_w, conv_b, conv_ln_w, conv_ln_b, w_pw_conv, w_out, norm2_w, w_ffn_in, w_ffn_out, loss_target, m_norm1_w, m_w_in, m_b_gate, m_q_norm_w, m_k_norm_w, m_w_o_attn, m_conv_w, m_conv_b, m_conv_ln_w, m_conv_ln_b, m_w_pw_conv, m_w_out, m_norm2_w, m_w_ffn_in, m_w_ffn_out, v_norm1_w, v_w_in, v_b_gate, v_q_norm_w, v_k_norm_w, v_w_o_attn, v_conv_w, v_conv_b, v_conv_ln_w, v_conv_ln_b, v_w_pw_conv, v_w_out, v_norm2_w, v_w_ffn_in, v_w_ffn_out):
    w = dict(norm1_w=norm1_w, w_in=w_in, b_gate=b_gate, q_norm_w=q_norm_w, k_norm_w=k_norm_w, w_o_attn=w_o_attn,
             conv_w=conv_w, conv_b=conv_b, conv_ln_w=conv_ln_w, conv_ln_b=conv_ln_b, w_pw_conv=w_pw_conv,
             w_out=w_out, norm2_w=norm2_w, w_ffn_in=w_ffn_in, w_ffn_out=w_ffn_out)
    m = dict(norm1_w=m_norm1_w, w_in=m_w_in, b_gate=m_b_gate, q_norm_w=m_q_norm_w, k_norm_w=m_k_norm_w,
             w_o_attn=m_w_o_attn, conv_w=m_conv_w, conv_b=m_conv_b, conv_ln_w=m_conv_ln_w,
             conv_ln_b=m_conv_ln_b, w_pw_conv=m_w_pw_conv, w_out=m_w_out, norm2_w=m_norm2_w,
             w_ffn_in=m_w_ffn_in, w_ffn_out=m_w_ffn_out)
    v = dict(norm1_w=v_norm1_w, w_in=v_w_in, b_gate=v_b_gate, q_norm_w=v_q_norm_w, k_norm_w=v_k_norm_w,
             w_o_attn=v_w_o_attn, conv_w=v_conv_w, conv_b=v_conv_b, conv_ln_w=v_conv_ln_w,
             conv_ln_b=v_conv_ln_b, w_pw_conv=v_w_pw_conv, w_out=v_w_out, norm2_w=v_norm2_w,
             w_ffn_in=v_w_ffn_in, w_ffn_out=v_w_ffn_out)
    cx, cy, cc = _mesh_pos()
    chip = 2 * cx + cy

    chip_arr = chip.reshape(1).astype(jnp.int32)
    pos_arr = jnp.stack([cc, chip]).astype(jnp.int32)
    bufs = {}
    for n in BIG:
        buf = _cast_into_slot(w[n][0], chip_arr, BF16, f"cast_{n}")
        bufs[n] = buf.reshape(N_CHIPS, 2, buf.shape[1] // 2, buf.shape[2])
    small_bufs = [_cast_into_slot(w[n][0], chip_arr, F32, f"slot_{n}") for n in ("conv_w", "b_gate")]
    w_in_buf, conv_w_buf, b_gate_buf = _comm_call(_gather_both_legs_rider([bufs["w_in"]], small_bufs),
                                                  "allgather_w_in")
    wts = dict(w_in=w_in_buf.reshape(N_CHIPS, -1, w_in_buf.shape[3]),
               conv_w=conv_w_buf.transpose(1, 0, 2).reshape(CONV_WIDTH, -1),
               b_gate=b_gate_buf.transpose(1, 0, 2).reshape(2, 1, -1),
               norm1_w=norm1_w, q_norm_w=q_norm_w, k_norm_w=k_norm_w, conv_b=conv_b, conv_ln_w=conv_ln_w,
               conv_ln_b=conv_ln_b, norm2_w=norm2_w)

    loss, grad_x, g, reduced, w_in_in_flight = _forward_backward(
        x[0], positions.reshape(-1, 1), loss_target[0], wts, [bufs[n] for n in LATE_GATHER], pos_arr)
    grads = {n: b.reshape(-1, b.shape[2]) for n, b in reduced.items()}

    w_in_in_flight, started = w_in_in_flight
    delta, new_m, new_v = {}, {}, {}
    for n in EARLY_REDUCE:
        grads[n], delta[n], new_m[n], new_v[n] = _adamw(w[n][0], grads[n], m[n][0], v[n][0], f"adamw_{n}",
                                                        after=[started])
    small_parts = [loss] + [g[n] for n in SMALL]
    small_shapes = [p.shape for p in small_parts]
    summed = _small_allreduce(_pack_small(small_parts), "small_allreduce",
                              after=[delta[n] for n in EARLY_REDUCE])
    reduced = _unpack_small(summed, small_shapes)
    loss_total = reduced[0].reshape(())
    for n, r in zip(SMALL, reduced[1:]):
        grads[n] = r
    ch_shard = conv_w.shape[2]
    grads["conv_w"] = lax.dynamic_slice_in_dim(grads["conv_w"], chip * ch_shard, ch_shard, axis=1)
    d_shard = b_gate.shape[2]
    grads["b_gate"] = lax.dynamic_slice_in_dim(grads["b_gate"], chip * d_shard, d_shard, axis=1)

    (by_chip_w_in,) = _split_wait(w_in_in_flight, after=[delta[n] for n in EARLY_REDUCE] + [summed],
                                  name="grads_w_in_exchange_wait")
    half_w_in = _sum_chips(by_chip_w_in, pos_arr, "grads_chip_sum_w_in")
    (shard_w_in,) = _comm_call(_pair_gather_rider([half_w_in]), "grads_pair_gather_w_in")
    grads["w_in"], delta["w_in"], new_m["w_in"], new_v["w_in"] = _adamw(
        w["w_in"][0], shard_w_in.reshape(-1, shard_w_in.shape[2]), m["w_in"][0], v["w_in"][0], "adamw_w_in")
    flat2 = lambda a: a.reshape(-1, a.shape[-1])
    d_s, m_s, v_s = _adamw_small([flat2(w[n]) for n in SMALL], [flat2(grads[n]) for n in SMALL],
                                 [flat2(m[n]) for n in SMALL], [flat2(v[n]) for n in SMALL], "adamw_small")
    for i, n in enumerate(SMALL):
        delta[n], new_m[n], new_v[n] = d_s[i], m_s[i], v_s[i]

    shaped = lambda d, n: d[n].reshape(w[n].shape)
    return (loss_total, grad_x[None], *[shaped(grads, n) for n in ORDER], *[shaped(delta, n) for n in ORDER],
            *[shaped(new_m, n) for n in ORDER], *[shaped(new_v, n) for n in ORDER])
```

```python
import functools

import numpy as np
import jax
import jax.numpy as jnp
from jax import lax
from jax.experimental import pallas as pl
from jax.experimental.pallas import tpu as pltpu

F32 = jnp.float32
BF16 = jnp.bfloat16
MESH = pl.DeviceIdType.MESH
ANY = pl.BlockSpec(memory_space=pl.ANY)

HEAD_DIM = 64
N_SLOT_HEADS = 8
DILATIONS = (1, 4, 16)
HALF_SPAN = 64
ROPE_THETA = 500000.0
ROT_DIM = 16
CONV_WIDTH = 31
EPS = 1e-6
NEG_INF = -1e30
ADAM_LR, ADAM_B1, ADAM_B2, ADAM_EPS, ADAM_WD, ADAM_STEP = 0.001, 0.9, 0.999, 1e-08, 0.01, 10

LANES = 128
QBLK = 128
KWIN = QBLK + 2 * HALF_SPAN
VMEM_LIMIT = 48 * 1024 * 1024
N_CHIPS = 4


def _params(**kw):
    return pltpu.CompilerParams(vmem_limit_bytes=VMEM_LIMIT, **kw)


class _Rider:
    def __init__(self, operands, out_shapes, aliases, scratch, start, wait):
        self.operands, self.out_shapes, self.aliases = list(operands), list(out_shapes), dict(aliases)
        self.scratch, self.start, self.wait = list(scratch), start, wait


def _pallas(body, *, name, grid, in_specs, out_specs, out_shape, operands, scratch_shapes=(), aliases=None,
            rider=None, after=()):
    single = not isinstance(out_specs, (list, tuple))
    out_specs_l = [out_specs] if single else list(out_specs)
    out_shape_l = [out_shape] if single else list(out_shape)
    aliases = dict(aliases or {})

    def call(fn, all_in_specs, all_out_specs, all_out_shape, all_scratch, all_aliases, all_operands):
        return pl.pallas_call(
            fn, name=name, grid=grid, in_specs=all_in_specs, out_specs=all_out_specs, out_shape=all_out_shape,
            scratch_shapes=all_scratch, input_output_aliases=all_aliases, compiler_params=_params(),
        )(*all_operands)

    if rider is None:
        n_main = len(in_specs)

        def ordered(*refs):
            body(*refs[:n_main], *refs[n_main + len(after):])

        res = call(ordered if after else body, list(in_specs) + [ANY] * len(after), out_specs_l, out_shape_l,
                   list(scratch_shapes), aliases, list(operands) + list(after))
        return res[0] if single else res
    assert not after
    n_in, n_rin = len(in_specs), len(rider.operands)
    n_out, n_rout = len(out_specs_l), len(rider.out_shapes)
    n_sc = len(scratch_shapes)

    def wrapped(*refs):
        main_in, r_in = refs[:n_in], refs[n_in:n_in + n_rin]
        o0 = n_in + n_rin
        main_out, r_out = refs[o0:o0 + n_out], refs[o0 + n_out:o0 + n_out + n_rout]
        s0 = o0 + n_out + n_rout
        main_sc, r_sc = refs[s0:s0 + n_sc], refs[s0 + n_sc:]
        ids = [pl.program_id(d) for d in range(len(grid))]
        first = functools.reduce(jnp.logical_and, [i == 0 for i in ids])
        last = functools.reduce(jnp.logical_and, [i == n - 1 for i, n in zip(ids, grid)])

        @pl.when(first)
        def _():
            rider.start(r_in, r_out, r_sc)

        body(*main_in, *main_out, *main_sc)

        @pl.when(last)
        def _():
            rider.wait(r_in, r_out, r_sc)

    for src, dst in rider.aliases.items():
        aliases[n_in + src] = n_out + dst
    res = call(wrapped, list(in_specs) + [ANY] * n_rin, out_specs_l + [ANY] * n_rout,
               out_shape_l + rider.out_shapes, list(scratch_shapes) + rider.scratch, aliases,
               list(operands) + rider.operands)
    main = res[:n_out]
    return (main[0] if single else main), res[n_out:]


def _matmul(a, b, *, mode, tm, tn, tk, out_dtype, name, b_blocked=False,
            out_blocked=None, rider=None, after=()):
    a_shape = a.shape
    if mode == "nn":
        m_dim, k_dim = a_shape
        n_dim = b.shape[0] * b.shape[2] if b_blocked else b.shape[1]
        rows, cols, red = m_dim, n_dim, k_dim
    elif mode == "nt":
        m_dim, n_dim = a_shape
        k_dim = b.shape[1] if b_blocked else b.shape[0]
        rows, cols, red = m_dim, k_dim, n_dim
    else:
        m_dim, k_dim = a_shape
        n_dim = b.shape[1]
        rows, cols, red = k_dim, n_dim, m_dim
    assert rows % tm == 0 and cols % tn == 0 and red % tk == 0, (name, rows, cols, red)
    ni, nj, nk = rows // tm, cols // tn, red // tk

    if mode == "nn":
        a_spec = pl.BlockSpec((tm, tk), lambda i, j, k: (i, k))
        if b_blocked:
            per = b.shape[2] // tn
            b_spec = pl.BlockSpec((None, tk, tn), lambda i, j, k: (j // per, k, j % per))
        else:
            b_spec = pl.BlockSpec((tk, tn), lambda i, j, k: (k, j))
        dims = (((1,), (0,)), ((), ()))
    elif mode == "nt":
        a_spec = pl.BlockSpec((tm, tk), lambda i, j, k: (i, k))
        if b_blocked:
            per = b.shape[2] // tk
            b_spec = pl.BlockSpec((None, tn, tk), lambda i, j, k: (k // per, j, k % per))
        else:
            b_spec = pl.BlockSpec((tn, tk), lambda i, j, k: (j, k))
        dims = (((1,), (1,)), ((), ()))
    else:
        a_spec = pl.BlockSpec((tk, tm), lambda i, j, k: (k, i))
        b_spec = pl.BlockSpec((tk, tn), lambda i, j, k: (k, j))
        dims = (((0,), (0,)), ((), ()))

    if out_blocked:
        per_o = (cols // out_blocked) // tn
        out_spec = pl.BlockSpec((None, tm, tn), lambda i, j, k: (j // per_o, i, j % per_o))
        out_shape = jax.ShapeDtypeStruct((out_blocked, rows, cols // out_blocked), out_dtype)
    else:
        out_spec = pl.BlockSpec((tm, tn), lambda i, j, k: (i, j))
        out_shape = jax.ShapeDtypeStruct((rows, cols), out_dtype)

    def body(a_ref, b_ref, o_ref, *acc):
        prod = lax.dot_general(a_ref[...], b_ref[...], dims, preferred_element_type=F32)
        if nk == 1:
            o_ref[...] = prod.astype(out_dtype)
        else:
            acc_ref, = acc
            k = pl.program_id(2)

            @pl.when(k == 0)
            def _():
                acc_ref[...] = prod

            @pl.when(k > 0)
            def _():
                acc_ref[...] += prod

            @pl.when(k == nk - 1)
            def _():
                o_ref[...] = acc_ref[...].astype(out_dtype)

    scratch = [pltpu.VMEM((tm, tn), F32)] if nk > 1 else []
    return _pallas(body, name=name, grid=(ni, nj, nk), in_specs=[a_spec, b_spec], out_specs=out_spec,
                   out_shape=out_shape, operands=[a, b], scratch_shapes=scratch, rider=rider, after=after)


def _rmsnorm_fwd(x, w, name):
    s, d = x.shape
    tm = 256

    def body(x_ref, w_ref, o_ref):
        xv = x_ref[...]
        rstd = lax.rsqrt(jnp.mean(xv * xv, axis=-1, keepdims=True) + EPS)
        o_ref[...] = (xv * rstd * w_ref[...]).astype(BF16)

    return pl.pallas_call(
        body, name=name, grid=(s // tm,),
        in_specs=[pl.BlockSpec((tm, d), lambda i: (i, 0)), pl.BlockSpec((1, d), lambda i: (0, 0))],
        out_specs=pl.BlockSpec((tm, d), lambda i: (i, 0)),
        out_shape=jax.ShapeDtypeStruct((s, d), BF16), compiler_params=_params(),
    )(x, w)


def _rmsnorm_bwd(dh, x, w, dres, name, rider=None):
    s, d = x.shape
    tm = 256

    def body(dh_ref, x_ref, w_ref, dres_ref, dx_ref, dxb_ref, dw_ref):
        xv = x_ref[...]
        rstd = lax.rsqrt(jnp.mean(xv * xv, axis=-1, keepdims=True) + EPS)
        xhat = xv * rstd
        dhv = dh_ref[...]
        g = dhv * w_ref[...]
        dx = rstd * (g - xhat * jnp.mean(g * xhat, axis=-1, keepdims=True)) + dres_ref[...]
        dx_ref[...] = dx
        dxb_ref[...] = dx.astype(BF16)
        part = jnp.sum(dhv * xhat, axis=0, keepdims=True)

        @pl.when(pl.program_id(0) == 0)
        def _():
            dw_ref[...] = part

        @pl.when(pl.program_id(0) > 0)
        def _():
            dw_ref[...] += part

    row = pl.BlockSpec((tm, d), lambda i: (i, 0))
    vec = pl.BlockSpec((1, d), lambda i: (0, 0))
    return _pallas(
        body, name=name, grid=(s // tm,), in_specs=[row, row, vec, row], out_specs=[row, row, vec],
        out_shape=[jax.ShapeDtypeStruct((s, d), F32), jax.ShapeDtypeStruct((s, d), BF16),
                   jax.ShapeDtypeStruct((1, d), F32)],
        operands=[dh, x, w, dres], rider=rider)


def _rope_consts():
    lane = np.arange(LANES)
    in_head = lane % HEAD_DIM
    inv_freq = ROPE_THETA ** (-jnp.arange(0, ROT_DIM, 2, dtype=F32) / ROT_DIM)
    invf = jnp.where(jnp.asarray(in_head < ROT_DIM), jnp.tile(inv_freq, LANES // (ROT_DIM // 2)), 0.0)
    m_a = np.where(in_head < ROT_DIM // 2, -1.0, 0.0).astype(np.float32)
    m_b = np.where((in_head >= ROT_DIM // 2) & (in_head < ROT_DIM), 1.0, 0.0).astype(np.float32)
    block_diag = (lane[:, None] // HEAD_DIM == lane[None, :] // HEAD_DIM).astype(np.float32)
    return (invf.reshape(1, LANES).astype(F32), jnp.asarray(m_a).reshape(1, LANES),
            jnp.asarray(m_b).reshape(1, LANES), jnp.asarray(block_diag, dtype=BF16))


def _head_sums(v, bd):
    hi = v.astype(BF16)
    lo = (v - hi.astype(F32)).astype(BF16)
    return jnp.dot(hi, bd, preferred_element_type=F32) + jnp.dot(lo, bd, preferred_element_type=F32)


def _qk_fwd(proj, pos_col, qw2, kw2, consts, name, rider=None):
    s = proj.shape[0]
    width = 3 * N_SLOT_HEADS * HEAD_DIM
    tm = 128
    invf, m_a, m_b, bd = consts
    scale = HEAD_DIM ** -0.5

    def body(q_ref, k_ref, pos_ref, qw_ref, kw_ref, invf_ref, ma_ref, mb_ref, bd_ref, qo_ref, ko_ref):
        ang = pos_ref[...].astype(F32) * invf_ref[...]
        cos = jnp.cos(ang)
        sin = jnp.sin(ang)
        s_a = sin * ma_ref[...]
        s_b = sin * mb_ref[...]
        bdv = bd_ref[...]
        for src, w_ref, dst, sc in ((q_ref, qw_ref, qo_ref, scale), (k_ref, kw_ref, ko_ref, 1.0)):
            for cb in range(width // LANES):
                cols = slice(cb * LANES, (cb + 1) * LANES)
                t = src[:, cols]
                rstd = lax.rsqrt(_head_sums(t * t, bdv) * (1.0 / HEAD_DIM) + EPS)
                y = t * rstd * w_ref[...]
                r = y * cos + pltpu.roll(y, LANES - 8, axis=1) * s_a + pltpu.roll(y, 8, axis=1) * s_b
                dst[:, cols] = r * sc if sc != 1.0 else r

    vec = pl.BlockSpec((1, LANES), lambda i: (0, 0))
    return _pallas(
        body, name=name, grid=(s // tm,),
        in_specs=[pl.BlockSpec((tm, width), lambda i: (i, 0)), pl.BlockSpec((tm, width), lambda i: (i, 1)),
                  pl.BlockSpec((tm, 1), lambda i: (i, 0)), vec, vec, vec, vec, vec,
                  pl.BlockSpec((LANES, LANES), lambda i: (0, 0))],
        out_specs=[pl.BlockSpec((tm, width), lambda i: (i, 0))] * 2,
        out_shape=[jax.ShapeDtypeStruct((s, width), F32)] * 2,
        operands=[proj, proj, pos_col, qw2, kw2, invf, m_a, m_b, bd], rider=rider)


def _qk_bwd(dqn, dkn, dv, da, db, dgl, proj, pos_col, qw2, kw2, consts, name, rider=None):
    s = proj.shape[0]
    width = 3 * N_SLOT_HEADS * HEAD_DIM
    ch = da.shape[1]
    gate_w = dgl.shape[1]
    out_w = 3 * width + 2 * ch + gate_w
    assert out_w == proj.shape[1]
    tm = 128
    invf, m_a, m_b, bd = consts
    scale = HEAD_DIM ** -0.5

    def body(dq_ref, dk_ref, dv_ref, da_ref, db_ref, dgl_ref, q_ref, k_ref, pos_ref, qw_ref, kw_ref,
             invf_ref, ma_ref, mb_ref, bd_ref, out_ref, dqw_ref, dkw_ref):
        ang = pos_ref[...].astype(F32) * invf_ref[...]
        cos = jnp.cos(ang)
        sin = jnp.sin(ang)
        s_a = sin * ma_ref[...]
        s_b = sin * mb_ref[...]
        bdv = bd_ref[...]
        first = pl.program_id(0) == 0
        for src, dsrc, w_ref, col0, dw_ref, sc in ((q_ref, dq_ref, qw_ref, 0, dqw_ref, scale),
                                                   (k_ref, dk_ref, kw_ref, width, dkw_ref, 1.0)):
            dw_acc = jnp.zeros((1, LANES), F32)
            for cb in range(width // LANES):
                cols = slice(cb * LANES, (cb + 1) * LANES)
                t = src[:, cols]
                dr = dsrc[:, cols]
                if sc != 1.0:
                    dr = dr * sc
                dy = dr * cos + pltpu.roll(dr * s_a, 8, axis=1) + pltpu.roll(dr * s_b, LANES - 8, axis=1)
                rstd = lax.rsqrt(_head_sums(t * t, bdv) * (1.0 / HEAD_DIM) + EPS)
                xhat = t * rstd
                g = dy * w_ref[...]
                dt = rstd * (g - xhat * (_head_sums(g * xhat, bdv) * (1.0 / HEAD_DIM)))
                out_ref[:, col0 + cb * LANES: col0 + (cb + 1) * LANES] = dt.astype(BF16)
                dw_acc = dw_acc + jnp.sum(dy * xhat, axis=0, keepdims=True)
            dw_acc = dw_acc + pltpu.roll(dw_acc, HEAD_DIM, axis=1)

            @pl.when(first)
            def _(dw_ref=dw_ref, dw_acc=dw_acc):
                dw_ref[...] = dw_acc

            @pl.when(jnp.logical_not(first))
            def _(dw_ref=dw_ref, dw_acc=dw_acc):
                dw_ref[...] += dw_acc
        out_ref[:, 2 * width: 3 * width] = dv_ref[...].astype(BF16)
        out_ref[:, 3 * width: 3 * width + ch] = da_ref[...]
        out_ref[:, 3 * width + ch: 3 * width + 2 * ch] = db_ref[...]
        out_ref[:, 3 * width + 2 * ch: out_w] = dgl_ref[...]

    vec = pl.BlockSpec((1, LANES), lambda i: (0, 0))
    blk = lambda c: pl.BlockSpec((tm, width), lambda i: (i, c))
    cblk = pl.BlockSpec((tm, ch), lambda i: (i, 0))
    return _pallas(
        body, name=name, grid=(s // tm,),
        in_specs=[blk(0), blk(0), blk(0), cblk, cblk, pl.BlockSpec((tm, gate_w), lambda i: (i, 0)),
                  blk(0), blk(1), pl.BlockSpec((tm, 1), lambda i: (i, 0)), vec, vec, vec, vec, vec,
                  pl.BlockSpec((LANES, LANES), lambda i: (0, 0))],
        out_specs=[pl.BlockSpec((tm, out_w), lambda i: (i, 0)), vec, vec],
        out_shape=[jax.ShapeDtypeStruct((s, out_w), BF16)] + [jax.ShapeDtypeStruct((1, LANES), F32)] * 2,
        operands=[dqn, dkn, dv, da, db, dgl, proj, proj, pos_col, qw2, kw2, invf, m_a, m_b, bd],
        rider=rider)


def _row_chunks(n_rows, fn, chunk=256):
    def step(i, c):
        fn(pl.ds(pl.multiple_of(i * chunk, chunk), chunk))
        return c
    lax.fori_loop(0, n_rows // chunk, step, 0)


def _to_residue_major(dst, src, s, d, dst_off=0, cast=None):
    seq = s // d
    for r in range(d):
        v = src[...] if d == 1 else src[pl.ds(r, seq, stride=d), :]
        dst[dst_off + r * seq: dst_off + (r + 1) * seq, :] = v if cast is None else v.astype(cast)


def _from_residue_major(dst, src, s, d, src_off=0):
    seq = s // d
    for r in range(d):
        v = src[src_off + r * seq: src_off + (r + 1) * seq, :]
        if d == 1:
            dst[...] = v
        else:
            dst[pl.ds(r, seq, stride=d), :] = v


def _band_bias():
    qi = lax.broadcasted_iota(jnp.int32, (QBLK, KWIN), 0)
    kj = lax.broadcasted_iota(jnp.int32, (QBLK, KWIN), 1)
    return jnp.where(jnp.abs(kj - HALF_SPAN - qi) <= HALF_SPAN, 0.0, NEG_INF).astype(F32)


def _range_bias(base, seq):
    kj = lax.broadcasted_iota(jnp.int32, (1, KWIN), 1)
    lo = (base & -seq) - base + HALF_SPAN
    return jnp.where((kj >= lo) & (kj < lo + seq), 0.0, NEG_INF).astype(F32)


ATTN_BLOCKS_PER_TRIP = 4


def _skewed_blocks(n_blk, produce, consume):
    per = ATTN_BLOCKS_PER_TRIP
    produce(0, 0)

    def trip(i, carry):
        for u in range(per):
            consume(per * i + u, u % 2)
            produce(per * i + u + 1, (u + 1) % 2)
        return carry

    n_trips = n_blk // per - 1
    lax.fori_loop(0, n_trips, trip, 0)
    for b in range(per * n_trips, n_blk):
        consume(b, b % 2)
        if b + 1 < n_blk:
            produce(b + 1, (b + 1) % 2)


def _block_base(b):
    return b * QBLK if isinstance(b, int) else pl.multiple_of(b * QBLK, QBLK)


def _attn_fwd(qn, kn, proj, name, rider=None):
    s = qn.shape[0]
    n_pairs = N_SLOT_HEADS * HEAD_DIM // LANES
    v_col0 = 2 * qn.shape[1] // LANES
    nt_dims = (((1,), (1,)), ((), ()))

    def body(q_ref, k_ref, v_ref, attn_ref, lse_ref, attn_b_ref, q_rm, k_rm, v_rm, acc_rm, m_rm, l_rm,
             acc_p, m_p, l_p, m_run, l_run, acc_run, band, s_buf, m_buf):
        g = pl.program_id(1)
        zpad = jnp.zeros((HALF_SPAN, LANES), BF16)
        k_rm[0:HALF_SPAN, :] = zpad
        k_rm[s + HALF_SPAN: s + 2 * HALF_SPAN, :] = zpad
        v_rm[0:HALF_SPAN, 0:LANES] = zpad
        v_rm[s + HALF_SPAN: s + 2 * HALF_SPAN, 0:LANES] = zpad

        def ones_rows(rows):
            v_rm[pl.ds(rows.start, rows.size), LANES:2 * LANES] = jnp.ones((rows.size, LANES), BF16)

        _row_chunks(s + 2 * HALF_SPAN, ones_rows, chunk=2 * HALF_SPAN)
        band[...] = _band_bias()
        lane = lax.broadcasted_iota(jnp.int32, (QBLK, LANES), 1)
        low = lane < HEAD_DIM
        n_blk = s // QBLK

        for gi, d in enumerate(DILATIONS):
            @pl.when(g == gi)
            def _(gi=gi, d=d):
                seq = s // d
                _to_residue_major(q_rm, q_ref, s, d, cast=BF16)
                _to_residue_major(k_rm, k_ref, s, d, dst_off=HALF_SPAN, cast=BF16)
                _to_residue_major(v_rm.at[:, 0:LANES], v_ref, s, d, dst_off=HALF_SPAN, cast=BF16)

                def scores(b, slot):
                    base = _block_base(b)
                    q = q_rm[pl.ds(base, QBLK), :]
                    zero = jnp.zeros_like(q)
                    q2 = jnp.concatenate([jnp.where(low, q, zero), jnp.where(low, zero, q)], axis=0)
                    sc = lax.dot_general(q2, k_rm[pl.ds(base, KWIN), :], nt_dims, preferred_element_type=F32)
                    bias = band[...] + _range_bias(base, seq)
                    for hh in range(2):
                        rows = slice(hh * QBLK, (hh + 1) * QBLK)
                        sh = sc[rows, :] + bias
                        s_buf[slot, rows, :] = sh
                        m_buf[slot, rows, :] = jnp.broadcast_to(jnp.max(sh, axis=-1, keepdims=True), (QBLK, LANES))

                def outputs(b, slot):
                    base = _block_base(b)
                    sv = s_buf[slot]
                    mb = m_buf[slot]
                    p = jnp.exp(jnp.concatenate([sv[:, 0:LANES] - mb, sv[:, LANES:2 * LANES] - mb], axis=1))
                    pv = jnp.dot(p.astype(BF16), v_rm[pl.ds(base, KWIN), :], preferred_element_type=F32)
                    rows = pl.ds(base, QBLK)
                    acc_rm[rows, :] = jnp.where(low, pv[0:QBLK, 0:LANES], pv[QBLK:2 * QBLK, 0:LANES])
                    l_rm[rows, :] = jnp.where(low, pv[0:QBLK, LANES:2 * LANES], pv[QBLK:2 * QBLK, LANES:2 * LANES])
                    m_rm[rows, :] = jnp.where(low, mb[0:QBLK, :], mb[QBLK:2 * QBLK, :])

                _skewed_blocks(n_blk, scores, outputs)
                if d == 1:
                    src = (acc_rm, m_rm, l_rm)
                else:
                    for dst_, src_ in ((acc_p, acc_rm), (m_p, m_rm), (l_p, l_rm)):
                        _from_residue_major(dst_, src_, s, d)
                    src = (acc_p, m_p, l_p)

                def combine(rows):
                    a_g, m_g, l_g = src[0][rows, :], src[1][rows, :], src[2][rows, :]
                    if gi == 0:
                        m_new, l_new, a_new = m_g, l_g, a_g
                    else:
                        m_old = m_run[rows, :]
                        m_new = jnp.maximum(m_old, m_g)
                        w_old = jnp.exp(m_old - m_new)
                        w_g = jnp.exp(m_g - m_new)
                        l_new = l_run[rows, :] * w_old + l_g * w_g
                        a_new = acc_run[rows, :] * w_old + a_g * w_g
                    if gi == len(DILATIONS) - 1:
                        out = a_new / l_new
                        attn_ref[rows, :] = out
                        attn_b_ref[rows, :] = out.astype(BF16)
                        lse_ref[rows, :] = m_new + jnp.log(l_new)
                    else:
                        m_run[rows, :] = m_new
                        l_run[rows, :] = l_new
                        acc_run[rows, :] = a_new

                _row_chunks(s, combine)

    qk_spec = pl.BlockSpec((s, LANES), lambda hp, g: (0, g * n_pairs + hp))
    v_spec = pl.BlockSpec((s, LANES), lambda hp, g: (0, v_col0 + g * n_pairs + hp))
    o_spec = pl.BlockSpec((s, LANES), lambda hp, g: (0, hp))
    f32buf = pltpu.VMEM((s, LANES), F32)
    return _pallas(
        body, name=name, grid=(n_pairs, len(DILATIONS)), in_specs=[qk_spec, qk_spec, v_spec],
        out_specs=[o_spec, o_spec, o_spec],
        out_shape=[jax.ShapeDtypeStruct((s, n_pairs * LANES), F32)] * 2
        + [jax.ShapeDtypeStruct((s, n_pairs * LANES), BF16)],
        operands=[qn, kn, proj],
        scratch_shapes=[pltpu.VMEM((s, LANES), BF16), pltpu.VMEM((s + 2 * HALF_SPAN, LANES), BF16),
                        pltpu.VMEM((s + 2 * HALF_SPAN, 2 * LANES), BF16)] + [f32buf] * 9
        + [pltpu.VMEM((QBLK, KWIN), F32), pltpu.VMEM((2, 2 * QBLK, KWIN), F32),
           pltpu.VMEM((2, 2 * QBLK, LANES), F32)],
        rider=rider)


def _attn_bwd(qn, kn, proj, dattn, attn, lse, bd, name, rider=None):
    s = qn.shape[0]
    n_pairs = N_SLOT_HEADS * HEAD_DIM // LANES
    v_col0 = 2 * qn.shape[1] // LANES
    nt_dims = (((1,), (1,)), ((), ()))
    tn_dims = (((0,), (0,)), ((), ()))
    spad = s + 2 * HALF_SPAN

    def body(q_ref, k_ref, v_ref, do_ref, o_ref, lse_ref, bd_ref, dq_ref, dk_ref, dv_ref,
             q_rm, k_rm, v_rm, do_rm, lse0_rm, lse1_rm, dd0_rm, dd1_rm, dq_rm, dk_rm, dv_rm,
             lse0_p, lse1_p, dd0_p, dd1_p, band, p_buf, ds_buf):
        g = pl.program_id(1)
        zpad = jnp.zeros((HALF_SPAN, LANES), BF16)
        for buf in (k_rm, v_rm):
            buf[0:HALF_SPAN, :] = zpad
            buf[s + HALF_SPAN: spad, :] = zpad
        zf = jnp.zeros((HALF_SPAN, LANES), F32)
        for buf in (dk_rm, dv_rm):
            buf[0:HALF_SPAN, :] = zf
            buf[s + HALF_SPAN: spad, :] = zf
        band[...] = _band_bias()

        def clear(rows):
            z = jnp.zeros((rows.size, LANES), F32)
            dk_rm[pl.ds(rows.start + HALF_SPAN, rows.size), :] = z
            dv_rm[pl.ds(rows.start + HALF_SPAN, rows.size), :] = z

        _row_chunks(s, clear)

        def prepare(rows):
            lo = lax.broadcasted_iota(jnp.int32, (rows.size, LANES), 1) < HEAD_DIM
            dsum = _head_sums(do_ref[rows, :] * o_ref[rows, :], bd_ref[...])
            dswap = pltpu.roll(dsum, HEAD_DIM, axis=1)
            dd0_p[rows, :] = jnp.where(lo, dsum, dswap)
            dd1_p[rows, :] = jnp.where(lo, dswap, dsum)
            lv = lse_ref[rows, :]
            lswap = pltpu.roll(lv, HEAD_DIM, axis=1)
            lse0_p[rows, :] = jnp.where(lo, lv, lswap)
            lse1_p[rows, :] = jnp.where(lo, lswap, lv)

        @pl.when(g == 0)
        def _():
            _row_chunks(s, prepare)
        lane = lax.broadcasted_iota(jnp.int32, (QBLK, LANES), 1)
        low = lane < HEAD_DIM
        n_blk = s // QBLK

        def stacked(ref, rows):
            val = ref[rows, :]
            zero = jnp.zeros_like(val)
            return jnp.concatenate([jnp.where(low, val, zero), jnp.where(low, zero, val)], axis=0)

        for gi, d in enumerate(DILATIONS):
            @pl.when(g == gi)
            def _(d=d):
                seq = s // d
                _to_residue_major(q_rm, q_ref, s, d, cast=BF16)
                _to_residue_major(k_rm, k_ref, s, d, dst_off=HALF_SPAN, cast=BF16)
                _to_residue_major(v_rm, v_ref, s, d, dst_off=HALF_SPAN, cast=BF16)
                _to_residue_major(do_rm, do_ref, s, d, cast=BF16)
                for dst_, src_ in ((lse0_rm, lse0_p), (lse1_rm, lse1_p), (dd0_rm, dd0_p), (dd1_rm, dd1_p)):
                    _to_residue_major(dst_, src_, s, d)

                def scores(b, slot):
                    base = _block_base(b)
                    rows = pl.ds(base, QBLK)
                    win = pl.ds(base, KWIN)
                    sc = lax.dot_general(stacked(q_rm, rows), k_rm[win, :], nt_dims, preferred_element_type=F32)
                    dp = lax.dot_general(stacked(do_rm, rows), v_rm[win, :], nt_dims, preferred_element_type=F32)
                    bias = band[...] + _range_bias(base, seq)
                    for hh, (lse_r, dd_r) in enumerate(((lse0_rm, dd0_rm), (lse1_rm, dd1_rm))):
                        r = slice(hh * QBLK, (hh + 1) * QBLK)
                        lse_h = lse_r[rows, :]
                        dd_h = dd_r[rows, :]
                        sh = sc[r, :] + bias
                        p = jnp.exp(jnp.concatenate([sh[:, 0:LANES] - lse_h, sh[:, LANES:KWIN] - lse_h], axis=1))
                        dph = dp[r, :]
                        ds = p * jnp.concatenate([dph[:, 0:LANES] - dd_h, dph[:, LANES:KWIN] - dd_h], axis=1)
                        p_buf[slot, r, :] = p.astype(BF16)
                        ds_buf[slot, r, :] = ds.astype(BF16)

                def grads(b, slot):
                    base = _block_base(b)
                    rows = pl.ds(base, QBLK)
                    win = pl.ds(base, KWIN)
                    p = p_buf[slot]
                    ds = ds_buf[slot]
                    dq2 = jnp.dot(ds, k_rm[win, :], preferred_element_type=F32)
                    dq_rm[rows, :] = jnp.where(low, dq2[0:QBLK, :], dq2[QBLK:2 * QBLK, :])
                    dk_rm[win, :] += lax.dot_general(ds, stacked(q_rm, rows), tn_dims, preferred_element_type=F32)
                    dv_rm[win, :] += lax.dot_general(p, stacked(do_rm, rows), tn_dims, preferred_element_type=F32)

                _skewed_blocks(n_blk, scores, grads)
                _from_residue_major(dq_ref, dq_rm, s, d)
                _from_residue_major(dk_ref, dk_rm, s, d, src_off=HALF_SPAN)
                _from_residue_major(dv_ref, dv_rm, s, d, src_off=HALF_SPAN)

    qk_spec = pl.BlockSpec((s, LANES), lambda hp, g: (0, g * n_pairs + hp))
    v_spec = pl.BlockSpec((s, LANES), lambda hp, g: (0, v_col0 + g * n_pairs + hp))
    o_spec = pl.BlockSpec((s, LANES), lambda hp, g: (0, hp))
    width = qn.shape[1]
    f32buf = pltpu.VMEM((s, LANES), F32)
    f32pad = pltpu.VMEM((spad, LANES), F32)
    return _pallas(
        body, name=name, grid=(n_pairs, len(DILATIONS)),
        in_specs=[qk_spec, qk_spec, v_spec, o_spec, o_spec, o_spec,
                  pl.BlockSpec((LANES, LANES), lambda hp, g: (0, 0))],
        out_specs=[qk_spec, qk_spec, qk_spec],
        out_shape=[jax.ShapeDtypeStruct((s, width), F32)] * 3,
        operands=[qn, kn, proj, dattn, attn, lse, bd],
        scratch_shapes=[pltpu.VMEM((s, LANES), BF16), pltpu.VMEM((spad, LANES), BF16),
                        pltpu.VMEM((spad, LANES), BF16), pltpu.VMEM((s, LANES), BF16),
                        f32buf, f32buf, f32buf, f32buf, f32buf, f32pad, f32pad,
                        f32buf, f32buf, f32buf, f32buf, pltpu.VMEM((QBLK, KWIN), F32),
                        pltpu.VMEM((2, 2 * QBLK, KWIN), BF16), pltpu.VMEM((2, 2 * QBLK, KWIN), BF16)],
        rider=rider)


CONV_PAD = 16


def _conv_fwd(proj, conv_w, conv_b, col0, name, rider=None):
    s = proj.shape[0]
    ch = conv_w.shape[1]
    nblk = ch // LANES
    a0 = col0 // LANES
    tr = 256
    shift = CONV_PAD - (CONV_WIDTH - 1) // 2

    def body(a_ref, b_ref, w_ref, bias_ref, u0_ref, uc_ref, pad):
        z = jnp.zeros((CONV_PAD, LANES), F32)
        pad[0:CONV_PAD, :] = z
        pad[s + CONV_PAD: s + 2 * CONV_PAD, :] = z

        def glu(rows):
            u0 = a_ref[rows, :] * jax.nn.sigmoid(b_ref[rows, :])
            u0_ref[rows, :] = u0
            pad[pl.ds(rows.start + CONV_PAD, rows.size), :] = u0

        _row_chunks(s, glu)
        for t in range(0, s, tr):
            acc = jnp.broadcast_to(bias_ref[...], (tr, LANES))
            for k in range(CONV_WIDTH):
                acc = acc + w_ref[k:k + 1, :] * pad[t + k + shift: t + k + shift + tr, :]
            uc_ref[t:t + tr, :] = acc

    return _pallas(
        body, name=name, grid=(nblk,),
        in_specs=[pl.BlockSpec((s, LANES), lambda c: (0, a0 + c)),
                  pl.BlockSpec((s, LANES), lambda c: (0, a0 + nblk + c)),
                  pl.BlockSpec((CONV_WIDTH, LANES), lambda c: (0, c)),
                  pl.BlockSpec((1, LANES), lambda c: (0, c))],
        out_specs=[pl.BlockSpec((s, LANES), lambda c: (0, c))] * 2,
        out_shape=[jax.ShapeDtypeStruct((s, ch), F32)] * 2, operands=[proj, proj, conv_w, conv_b],
        scratch_shapes=[pltpu.VMEM((s + 2 * CONV_PAD, LANES), F32)], rider=rider)


def _ln_silu_fwd(uc, ln_w, ln_b, name):
    s, ch = uc.shape
    tm = 256

    def body(u_ref, w_ref, b_ref, o_ref):
        u = u_ref[...]
        mu = jnp.mean(u, axis=-1, keepdims=True)
        xc = u - mu
        rstd = lax.rsqrt(jnp.mean(xc * xc, axis=-1, keepdims=True) + EPS)
        z = xc * rstd * w_ref[...] + b_ref[...]
        o_ref[...] = (z * jax.nn.sigmoid(z)).astype(BF16)

    row = pl.BlockSpec((tm, ch), lambda i: (i, 0))
    vec = pl.BlockSpec((1, ch), lambda i: (0, 0))
    return pl.pallas_call(
        body, name=name, grid=(s // tm,), in_specs=[row, vec, vec], out_specs=row,
        out_shape=jax.ShapeDtypeStruct((s, ch), BF16), compiler_params=_params(),
    )(uc, ln_w, ln_b)


def _ln_silu_bwd(du3, uc, ln_w, ln_b, name):
    s, ch = uc.shape
    tm = 256

    def body(d_ref, u_ref, w_ref, b_ref, du_ref, dw_ref, db_ref):
        u = u_ref[...]
        mu = jnp.mean(u, axis=-1, keepdims=True)
        xc = u - mu
        rstd = lax.rsqrt(jnp.mean(xc * xc, axis=-1, keepdims=True) + EPS)
        xhat = xc * rstd
        z = xhat * w_ref[...] + b_ref[...]
        sg = jax.nn.sigmoid(z)
        dz = d_ref[...] * (sg * (1.0 + z * (1.0 - sg)))
        dxh = dz * w_ref[...]
        du_ref[...] = rstd * (dxh - jnp.mean(dxh, axis=-1, keepdims=True)
                              - xhat * jnp.mean(dxh * xhat, axis=-1, keepdims=True))
        pw = jnp.sum(dz * xhat, axis=0, keepdims=True)
        pb = jnp.sum(dz, axis=0, keepdims=True)
        first = pl.program_id(0) == 0

        @pl.when(first)
        def _():
            dw_ref[...] = pw
            db_ref[...] = pb

        @pl.when(jnp.logical_not(first))
        def _():
            dw_ref[...] += pw
            db_ref[...] += pb

    row = pl.BlockSpec((tm, ch), lambda i: (i, 0))
    vec = pl.BlockSpec((1, ch), lambda i: (0, 0))
    return pl.pallas_call(
        body, name=name, grid=(s // tm,), in_specs=[row, row, vec, vec], out_specs=[row, vec, vec],
        out_shape=[jax.ShapeDtypeStruct((s, ch), F32), jax.ShapeDtypeStruct((1, ch), F32),
                   jax.ShapeDtypeStruct((1, ch), F32)],
        compiler_params=_params(),
    )(du3, uc, ln_w, ln_b)


def _conv_bwd(duc, u0, proj, conv_w, col0, name, rider=None):
    s = proj.shape[0]
    ch = conv_w.shape[1]
    nblk = ch // LANES
    a0 = col0 // LANES
    tr = 256
    half = (CONV_WIDTH - 1) // 2
    shift = CONV_PAD - half

    def body(duc_ref, u0_ref, a_ref, b_ref, w_ref, da_ref, db_ref, dw_ref, dbias_ref, pad_d, pad_u):
        z = jnp.zeros((CONV_PAD, LANES), F32)
        for buf in (pad_d, pad_u):
            buf[0:CONV_PAD, :] = z
            buf[s + CONV_PAD: s + 2 * CONV_PAD, :] = z

        def fill(rows):
            dst = pl.ds(rows.start + CONV_PAD, rows.size)
            pad_d[dst, :] = duc_ref[rows, :]
            pad_u[dst, :] = u0_ref[rows, :]

        _row_chunks(s, fill)
        dw_acc = [jnp.zeros((8, LANES), F32) for _ in range(CONV_WIDTH)]
        dbias_acc = jnp.zeros((8, LANES), F32)
        for t in range(0, s, tr):
            d_t = duc_ref[t:t + tr, :]
            dbias_acc = dbias_acc + jnp.sum(d_t.reshape(tr // 8, 8, LANES), axis=0)
            du0 = jnp.zeros((tr, LANES), F32)
            for k in range(CONV_WIDTH):
                du0 = du0 + w_ref[k:k + 1, :] * pad_d[t - k + half + CONV_PAD: t - k + half + CONV_PAD + tr, :]
                prod = d_t * pad_u[t + k + shift: t + k + shift + tr, :]
                dw_acc[k] = dw_acc[k] + jnp.sum(prod.reshape(tr // 8, 8, LANES), axis=0)
            av = a_ref[t:t + tr, :]
            sg = jax.nn.sigmoid(b_ref[t:t + tr, :])
            da_ref[t:t + tr, :] = (du0 * sg).astype(BF16)
            db_ref[t:t + tr, :] = (du0 * av * sg * (1.0 - sg)).astype(BF16)
        for k in range(CONV_WIDTH):
            dw_ref[k:k + 1, :] = jnp.sum(dw_acc[k], axis=0, keepdims=True)
        dbias_ref[...] = jnp.sum(dbias_acc, axis=0, keepdims=True)

    col = lambda off: pl.BlockSpec((s, LANES), lambda c: (0, off + c))
    return _pallas(
        body, name=name, grid=(nblk,),
        in_specs=[col(0), col(0), col(a0), col(a0 + nblk),
                  pl.BlockSpec((CONV_WIDTH, LANES), lambda c: (0, c))],
        out_specs=[col(0), col(0), pl.BlockSpec((CONV_WIDTH, LANES), lambda c: (0, c)),
                   pl.BlockSpec((1, LANES), lambda c: (0, c))],
        out_shape=[jax.ShapeDtypeStruct((s, ch), BF16)] * 2
        + [jax.ShapeDtypeStruct((CONV_WIDTH, ch), F32), jax.ShapeDtypeStruct((1, ch), F32)],
        operands=[duc, u0, proj, proj, conv_w],
        scratch_shapes=[pltpu.VMEM((s + 2 * CONV_PAD, LANES), F32)] * 2, rider=rider)


GATE_BLK = 512


def _gate_fwd(proj, bg, y_a, y_b, col0, name):
    s, d = y_a.shape
    tm = 256
    g0 = col0 // GATE_BLK
    nb = d // GATE_BLK

    def body(ga_ref, gb_ref, ba_ref, bb_ref, ya_ref, yb_ref, o_ref):
        g_a = jax.nn.sigmoid(ga_ref[...] + ba_ref[...])
        g_b = jax.nn.sigmoid(gb_ref[...] + bb_ref[...])
        o_ref[...] = (g_a * ya_ref[...] + g_b * yb_ref[...]).astype(BF16)

    act = pl.BlockSpec((tm, GATE_BLK), lambda i, j: (i, j))
    return pl.pallas_call(
        body, name=name, grid=(s // tm, nb),
        in_specs=[pl.BlockSpec((tm, GATE_BLK), lambda i, j: (i, g0 + j)),
                  pl.BlockSpec((tm, GATE_BLK), lambda i, j: (i, g0 + nb + j)),
                  pl.BlockSpec((None, 1, GATE_BLK), lambda i, j: (0, 0, j)),
                  pl.BlockSpec((None, 1, GATE_BLK), lambda i, j: (1, 0, j)), act, act],
        out_specs=act, out_shape=jax.ShapeDtypeStruct((s, d), BF16), compiler_params=_params(),
    )(proj, proj, bg, bg, y_a, y_b)


def _gate_bwd(d_mixed, proj, bg, y_a, y_b, col0, name, rider=None):
    s, d = y_a.shape
    tm = 256
    half = d // 2
    assert col0 % half == 0
    c0 = col0 // half

    def body(dm_ref, a0_ref, a1_ref, b0_ref, b1_ref, bias_ref, ya_ref, yb_ref, dgl_ref, dya_ref, dyb_ref, db_ref):
        dm = dm_ref[...]
        parts = []
        for br, (lo_ref, hi_ref, y_ref, dy_ref) in enumerate(((a0_ref, a1_ref, ya_ref, dya_ref),
                                                              (b0_ref, b1_ref, yb_ref, dyb_ref))):
            logits = jnp.concatenate([lo_ref[...], hi_ref[...]], axis=1)
            gate = jax.nn.sigmoid(logits + bias_ref[br])
            dy_ref[...] = (dm * gate).astype(BF16)
            dgl = dm * y_ref[...] * gate * (1.0 - gate)
            dgl_ref[:, br * d:(br + 1) * d] = dgl.astype(BF16)
            parts.append(jnp.sum(dgl, axis=0, keepdims=True))
        part = jnp.concatenate(parts, axis=0)
        first = pl.program_id(0) == 0

        @pl.when(first)
        def _():
            db_ref[...] = part

        @pl.when(jnp.logical_not(first))
        def _():
            db_ref[...] += part

    row = pl.BlockSpec((tm, d), lambda i: (i, 0))
    logit_blk = lambda k: pl.BlockSpec((tm, half), functools.partial(lambda i, k: (i, c0 + k), k=k))
    return _pallas(
        body, name=name, grid=(s // tm,),
        in_specs=[row, logit_blk(0), logit_blk(1), logit_blk(2), logit_blk(3),
                  pl.BlockSpec((2, 1, d), lambda i: (0, 0, 0)), row, row],
        out_specs=[pl.BlockSpec((tm, 2 * d), lambda i: (i, 0)), row, row, pl.BlockSpec((2, d), lambda i: (0, 0))],
        out_shape=[jax.ShapeDtypeStruct((s, 2 * d), BF16), jax.ShapeDtypeStruct((s, d), BF16),
                   jax.ShapeDtypeStruct((s, d), BF16), jax.ShapeDtypeStruct((2, d), F32)],
        operands=[d_mixed, proj, proj, proj, proj, bg, y_a, y_b], rider=rider)


def _ffn_in_swiglu(h2, w_blocked, name):
    s, k = h2.shape
    nblk, _, tn = w_blocked.shape
    ff = nblk // 2 * tn
    tm = 512

    def body(a_ref, wg_ref, wu_ref, g_ref, u_ref, act_ref):
        a = a_ref[...]
        gt = jnp.dot(a, wg_ref[...], preferred_element_type=F32)
        up = jnp.dot(a, wu_ref[...], preferred_element_type=F32)
        g_ref[...] = gt
        u_ref[...] = up
        act_ref[...] = (gt * jax.nn.sigmoid(gt) * up).astype(BF16)

    out = pl.BlockSpec((tm, tn), lambda i, j: (i, j))
    return pl.pallas_call(
        body, name=name, grid=(s // tm, nblk // 2),
        in_specs=[pl.BlockSpec((tm, k), lambda i, j: (i, 0)),
                  pl.BlockSpec((None, k, tn), lambda i, j: (j, 0, 0)),
                  pl.BlockSpec((None, k, tn), lambda i, j: (nblk // 2 + j, 0, 0))],
        out_specs=[out, out, out],
        out_shape=[jax.ShapeDtypeStruct((s, ff), F32), jax.ShapeDtypeStruct((s, ff), F32),
                   jax.ShapeDtypeStruct((s, ff), BF16)],
        compiler_params=_params(),
    )(h2, w_blocked, w_blocked)


def _swiglu_bwd(gate, up, d_act, name, rider=None):
    s, ff = gate.shape
    tm = 256

    def body(g_ref, u_ref, d_ref, o_ref):
        gt = g_ref[...]
        sg = jax.nn.sigmoid(gt)
        dv = d_ref[...]
        o_ref[:, 0:ff] = (dv * u_ref[...] * (sg * (1.0 + gt * (1.0 - sg)))).astype(BF16)
        o_ref[:, ff:2 * ff] = (dv * gt * sg).astype(BF16)

    row = pl.BlockSpec((tm, ff), lambda i: (i, 0))
    return _pallas(
        body, name=name, grid=(s // tm,), in_specs=[row, row, row],
        out_specs=pl.BlockSpec((tm, 2 * ff), lambda i: (i, 0)),
        out_shape=jax.ShapeDtypeStruct((s, 2 * ff), BF16), operands=[gate, up, d_act], rider=rider)


def _out_proj_rmsnorm(mixed, w_out, x, norm_w, name):
    s, k = mixed.shape
    d = w_out.shape[1]
    tm = 512

    def body(a_ref, w_ref, x_ref, nw_ref, x1_ref, h2_ref):
        x1 = x_ref[...] + jnp.dot(a_ref[...], w_ref[...], preferred_element_type=F32)
        x1_ref[...] = x1
        rstd = lax.rsqrt(jnp.mean(x1 * x1, axis=-1, keepdims=True) + EPS)
        h2_ref[...] = (x1 * rstd * nw_ref[...]).astype(BF16)

    row = pl.BlockSpec((tm, d), lambda i: (i, 0))
    return pl.pallas_call(
        body, name=name, grid=(s // tm,),
        in_specs=[pl.BlockSpec((tm, k), lambda i: (i, 0)), pl.BlockSpec((k, d), lambda i: (0, 0)), row,
                  pl.BlockSpec((1, d), lambda i: (0, 0))],
        out_specs=[row, row],
        out_shape=[jax.ShapeDtypeStruct((s, d), F32), jax.ShapeDtypeStruct((s, d), BF16)],
        compiler_params=_params(),
    )(mixed, w_out, x, norm_w)


def _ffn_out_loss(act, w_ffn_out, x1, target, name):
    s, k = act.shape
    d = w_ffn_out.shape[1]
    tm = 512

    def body(a_ref, w_ref, x1_ref, t_ref, dy_ref, dyb_ref, loss_ref, acc):
        y = x1_ref[...] + jnp.dot(a_ref[...], w_ref[...], preferred_element_type=F32)
        diff = y - t_ref[...]
        dy = diff * (1.0 / d)
        dy_ref[...] = dy
        dyb_ref[...] = dy.astype(BF16)
        part = jnp.sum((diff * diff).reshape(tm // 8, 8, d), axis=0)
        i = pl.program_id(0)

        @pl.when(i == 0)
        def _():
            acc[...] = part

        @pl.when(i > 0)
        def _():
            acc[...] += part

        @pl.when(i == pl.num_programs(0) - 1)
        def _():
            loss_ref[...] = (0.5 / d) * jnp.sum(jnp.sum(acc[...], axis=1, keepdims=True), axis=0, keepdims=True)

    row = pl.BlockSpec((tm, d), lambda i: (i, 0))
    return pl.pallas_call(
        body, name=name, grid=(s // tm,),
        in_specs=[pl.BlockSpec((tm, k), lambda i: (i, 0)), pl.BlockSpec((k, d), lambda i: (0, 0)), row, row],
        out_specs=[row, row, pl.BlockSpec((1, 1), lambda i: (0, 0))],
        out_shape=[jax.ShapeDtypeStruct((s, d), F32), jax.ShapeDtypeStruct((s, d), BF16),
                   jax.ShapeDtypeStruct((1, 1), F32)],
        scratch_shapes=[pltpu.VMEM((8, d), F32)], compiler_params=_params(),
    )(act, w_ffn_out, x1, target)


LATE_GATHER = ("w_o_attn", "w_pw_conv", "w_out", "w_ffn_in", "w_ffn_out")
EARLY_REDUCE = LATE_GATHER


def _blocks_by_half(g):
    if g.ndim == 2:
        g = g.reshape(N_CHIPS, g.shape[0] // N_CHIPS, g.shape[1])
    return g.reshape(N_CHIPS, 2, g.shape[1] // 2, g.shape[2])


def _forward_backward(x, pos_col, target, wts, late_bufs, pos_arr):
    wts = dict(wts)
    consts = _rope_consts()
    bd = consts[3]
    qw2 = jnp.tile(wts["q_norm_w"], (1, LANES // HEAD_DIM))
    kw2 = jnp.tile(wts["k_norm_w"], (1, LANES // HEAD_DIM))
    qkv_w = 3 * N_SLOT_HEADS * HEAD_DIM
    conv_col0 = 3 * qkv_w
    ch = wts["conv_w"].shape[1]
    gate_col0 = conv_col0 + 2 * ch

    h = _rmsnorm_fwd(x, wts["norm1_w"], "rms1_fwd")
    gathering, started = _split_start(_gather_ici_rider(late_bufs, []), "late_gather_start", after=[wts["w_in"]])
    proj = _matmul(h, wts["w_in"], mode="nn", tm=512, tn=1920, tk=1024, out_dtype=F32, name="mm_proj",
                   b_blocked=True, after=[started])
    qn, kn = _qk_fwd(proj, pos_col, qw2, kw2, consts, "qk_fwd")
    attn, lse, attn_b = _attn_fwd(qn, kn, proj, "attn_fwd")
    late_bufs, _ = _split_wait(gathering, after=[attn_b], name="late_gather_wait")
    (u0, uc), late_bufs = _conv_fwd(proj, wts["conv_w"], wts["conv_b"], conv_col0, "conv_fwd",
                                    rider=_gather_forward_rider(late_bufs))
    for n, buf in zip(LATE_GATHER, late_bufs):
        full = buf.reshape(N_CHIPS, -1, buf.shape[3])
        wts[n] = full.reshape(-1, full.shape[2]) if n in ROW_SHARDED else full
    y_a = _matmul(attn_b, wts["w_o_attn"], mode="nn", tm=1024, tn=256, tk=512, out_dtype=F32, name="mm_ya",
                  b_blocked=True)
    u3 = _ln_silu_fwd(uc, wts["conv_ln_w"], wts["conv_ln_b"], "ln_fwd")
    y_b = _matmul(u3, wts["w_pw_conv"], mode="nn", tm=1024, tn=256, tk=512, out_dtype=F32, name="mm_yb",
                  b_blocked=True)
    mixed = _gate_fwd(proj, wts["b_gate"], y_a, y_b, gate_col0, "gate_fwd")
    x1, h2 = _out_proj_rmsnorm(mixed, wts["w_out"], x, wts["norm2_w"], "mm_x1_rms2")
    gate, up, act = _ffn_in_swiglu(h2, wts["w_ffn_in"], "mm_gu_swiglu")
    dy, dy_b16, loss = _ffn_out_loss(act, wts["w_ffn_out"], x1, target, "mm_x2_loss")

    g = {}
    by_chip = {}

    def pair_add(n, blocks, received):
        return _add_own_half(blocks, received, pos_arr, f"grads_pair_add_{n}")

    d_act = _matmul(dy_b16, wts["w_ffn_out"], mode="nt", tm=512, tn=1408, tk=1024, out_dtype=F32, name="mm_dact")
    g_ffn_out = _blocks_by_half(
        _matmul(act, dy_b16, mode="tn", tm=1408, tn=1024, tk=2048, out_dtype=F32, name="mm_dwffnout"))
    dgu, (received,) = _swiglu_bwd(gate, up, d_act, "swiglu_bwd",
                                   rider=_pair_exchange_rider([g_ffn_out], halved=True))
    to_send, own = pair_add("w_ffn_out", g_ffn_out, received)
    dh2, (by_chip["w_ffn_out"],) = _matmul(
        dgu, wts["w_ffn_in"], mode="nt", tm=1024, tn=1024, tk=1408, out_dtype=F32, name="mm_dh2", b_blocked=True,
        rider=_chip_exchange_rider([to_send], [own]))
    g_ffn_in = _blocks_by_half(_matmul(h2, dgu, mode="tn", tm=512, tn=1408, tk=2048, out_dtype=F32,
                                       name="mm_dwffnin", out_blocked=N_CHIPS))
    exchanging, started = _split_start(_pair_exchange_rider([g_ffn_in], halved=True), "grads_ffn_in_pair_start")
    dx1, dx1_b16, g["norm2_w"] = _rmsnorm_bwd(dh2, x1, wts["norm2_w"], dy, "rms2_bwd")
    d_mixed = _matmul(dx1_b16, wts["w_out"], mode="nt", tm=512, tn=1024, tk=1024, out_dtype=F32, name="mm_dmixed",
                      after=[started])
    g["w_out"] = _matmul(mixed, dx1_b16, mode="tn", tm=512, tn=1024, tk=2048, out_dtype=F32, name="mm_dwout")
    dgl, dy_a, dy_b, g["b_gate"] = _gate_bwd(d_mixed, proj, wts["b_gate"], y_a, y_b, gate_col0, "gate_bwd")
    (received,), (g_ffn_in,) = _split_wait(exchanging, after=[dgl], name="grads_ffn_in_pair_wait")
    ffn_in_to_send, ffn_in_own = pair_add("w_ffn_in", g_ffn_in, received)
    dattn = _matmul(dy_a, wts["w_o_attn"], mode="nt", tm=1024, tn=512, tk=256, out_dtype=F32, name="mm_dattn",
                    b_blocked=True)
    g["w_o_attn"] = _matmul(attn_b, dy_a, mode="tn", tm=512, tn=256, tk=2048, out_dtype=F32, name="mm_dwo",
                            out_blocked=N_CHIPS)
    du3 = _matmul(dy_b, wts["w_pw_conv"], mode="nt", tm=1024, tn=512, tk=256, out_dtype=F32, name="mm_du3",
                  b_blocked=True)
    g["w_pw_conv"] = _matmul(u3, dy_b, mode="tn", tm=512, tn=256, tk=2048, out_dtype=F32, name="mm_dwpw",
                             out_blocked=N_CHIPS)
    duc, g["conv_ln_w"], g["conv_ln_b"] = _ln_silu_bwd(du3, uc, wts["conv_ln_w"], wts["conv_ln_b"], "ln_bwd")

    small3 = ("w_out", "w_o_attn", "w_pw_conv")
    g_small3 = [_blocks_by_half(g.pop(n)) for n in small3]
    (da, db, g["conv_w"], g["conv_b"]), received = _conv_bwd(
        duc, u0, proj, wts["conv_w"], conv_col0, "conv_bwd", rider=_pair_exchange_rider(g_small3, halved=True))
    sums3 = [pair_add(n, gb, rv) for n, gb, rv in zip(small3, g_small3, received)]
    (dqn, dkn, dv), (by_chip["w_ffn_in"],) = _attn_bwd(
        qn, kn, proj, dattn, attn, lse, bd, "attn_bwd",
        rider=_chip_exchange_rider([ffn_in_to_send], [ffn_in_own]))
    (dproj, dqw, dkw), exchanged3 = _qk_bwd(
        dqn, dkn, dv, da, db, dgl, proj, pos_col, qw2, kw2, consts, "qk_bwd",
        rider=_chip_exchange_rider([s[0] for s in sums3], [s[1] for s in sums3]))
    by_chip.update(zip(small3, exchanged3))
    halves = [_sum_chips(by_chip[n], pos_arr, f"grads_chip_sum_{n}") for n in EARLY_REDUCE]
    g["q_norm_w"] = dqw[:, :HEAD_DIM]
    g["k_norm_w"] = dkw[:, :HEAD_DIM]

    c = pos_arr[0]
    rh = h.shape[1] // 2
    h_sibling = lax.dynamic_slice_in_dim(h, (1 - c) * rh, rh, axis=1)
    h_own = lax.dynamic_slice_in_dim(h, c * rh, rh, axis=1)
    g_sibling, shards = _matmul(h_sibling, dproj, mode="tn", tm=rh, tn=1920, tk=2048, out_dtype=F32,
                                name="mm_dwin_sibling", out_blocked=N_CHIPS, rider=_pair_gather_rider(halves))
    reduced = dict(zip(EARLY_REDUCE, shards))
    exchanging, started = _split_start(_pair_exchange_rider([g_sibling], halved=False), "grads_w_in_pair_start")
    g_own = _matmul(h_own, dproj, mode="tn", tm=rh, tn=1920, tk=2048, out_dtype=F32, name="mm_dwin_own",
                    out_blocked=N_CHIPS, after=[started])
    (from_sibling,), _ = _split_wait(exchanging, after=[g_own], name="grads_w_in_pair_wait")
    to_send, own = _add_own_half(g_own, from_sibling, pos_arr, "grads_pair_add_w_in")
    in_flight, started = _split_start(_chip_exchange_rider([to_send], [own]), "grads_w_in_exchange_start")
    dh = _matmul(dproj, wts["w_in"], mode="nt", tm=1024, tn=1024, tk=1920, out_dtype=F32, name="mm_dh",
                 b_blocked=True, after=[started])
    grad_x, _, g["norm1_w"] = _rmsnorm_bwd(dh, x, wts["norm1_w"], dx1, "rms1_bwd")
    return loss, grad_x, g, reduced, (in_flight, started)


def _mesh_pos():
    return lax.axis_index("x"), lax.axis_index("y"), lax.axis_index("c")


def _other_chips(x, y):
    return [(1 - x, y), (x, 1 - y), (1 - x, 1 - y)]


def _cast_into_slot(shard, chip_arr, dtype, name):
    r, c = shard.shape
    tr = r // 2 if r % 32 == 0 else r

    def body(chip_ref, s_ref, o_ref):
        del chip_ref
        o_ref[...] = s_ref[...].astype(dtype)

    return pl.pallas_call(
        body, name=name,
        grid_spec=pltpu.PrefetchScalarGridSpec(
            num_scalar_prefetch=1, grid=(r // tr,),
            in_specs=[pl.BlockSpec((tr, c), lambda i, chip_ref: (i, 0))],
            out_specs=pl.BlockSpec((None, tr, c), lambda i, chip_ref: (chip_ref[0], i, 0))),
        out_shape=jax.ShapeDtypeStruct((N_CHIPS, r, c), dtype), compiler_params=_params(),
    )(chip_arr, shard)


def _gather_both_legs_rider(big, small):
    nb = len(big)
    n = nb + len(small)

    def part(bufs, a, slot, half):
        return bufs[a].at[slot, half] if a < nb else bufs[a].at[slot]

    def start(r_in, bufs, sems):
        x, y, c = _mesh_pos()
        me = 2 * x + y
        chips = _other_chips(x, y)
        for a in range(n):
            for k in range(3):
                px, py = chips[k]
                pltpu.make_async_remote_copy(
                    src_ref=part(bufs, a, me, c), dst_ref=part(bufs, a, me, c), send_sem=sems[0].at[a, k],
                    recv_sem=sems[1].at[a, k], device_id=(px, py, c), device_id_type=MESH).start()

    def wait(r_in, bufs, sems):
        x, y, c = _mesh_pos()
        me = 2 * x + y
        chips = _other_chips(x, y)
        sibling = (x, y, 1 - c)
        forwards = []
        for a in range(n):
            for k in range(3):
                px, py = chips[k]
                slot = 2 * px + py
                pltpu.make_async_remote_copy(
                    src_ref=part(bufs, a, slot, c), dst_ref=part(bufs, a, slot, c), send_sem=sems[0].at[a, k],
                    recv_sem=sems[1].at[a, k], device_id=(px, py, c), device_id_type=MESH).wait_recv()
                if a < nb:
                    fwd = pltpu.make_async_remote_copy(
                        src_ref=part(bufs, a, slot, c), dst_ref=part(bufs, a, slot, c), send_sem=sems[2].at[a, k],
                        recv_sem=sems[3].at[a, k], device_id=sibling, device_id_type=MESH)
                    fwd.start()
                    forwards.append(fwd)
        for a in range(nb):
            for k in range(3):
                px, py = chips[k]
                slot = 2 * px + py
                pltpu.make_async_remote_copy(
                    src_ref=part(bufs, a, slot, 1 - c), dst_ref=part(bufs, a, slot, 1 - c),
                    send_sem=sems[2].at[a, k], recv_sem=sems[3].at[a, k], device_id=sibling,
                    device_id_type=MESH).wait_recv()
        for a in range(n):
            for k in range(3):
                px, py = chips[k]
                pltpu.make_async_remote_copy(
                    src_ref=part(bufs, a, me, c), dst_ref=part(bufs, a, me, c), send_sem=sems[0].at[a, k],
                    recv_sem=sems[1].at[a, k], device_id=(px, py, c), device_id_type=MESH).wait_send()
        for fwd in forwards:
            fwd.wait_send()

    ops = list(big) + list(small)
    return _Rider(ops, [jax.ShapeDtypeStruct(o.shape, o.dtype) for o in ops], {i: i for i in range(n)},
                  [pltpu.SemaphoreType.DMA((n, 3)), pltpu.SemaphoreType.DMA((n, 3)),
                   pltpu.SemaphoreType.DMA((nb, 3)), pltpu.SemaphoreType.DMA((nb, 3))], start, wait)


def _comm_call(rider, name):
    def body():
        pass

    return _pallas(body, name=name, grid=(1,), in_specs=[], out_specs=[], out_shape=[], operands=[],
                   rider=rider)[1]


def _gather_ici_rider(big, small):
    nb = len(big)
    n = nb + len(small)

    def copies(bufs, sems):
        x, y, c = _mesh_pos()
        me = 2 * x + y
        part = lambda a, slot: bufs[a].at[slot, c] if a < nb else bufs[a].at[slot]
        out = []
        for a in range(n):
            for k, (px, py) in enumerate(_other_chips(x, y)):
                send = functools.partial(
                    pltpu.make_async_remote_copy,
                    src_ref=part(a, me), dst_ref=part(a, me), send_sem=sems[0].at[a, k],
                    recv_sem=sems[1].at[a, k], device_id=(px, py, c), device_id_type=MESH)
                recv = functools.partial(
                    pltpu.make_async_remote_copy,
                    src_ref=part(a, 2 * px + py), dst_ref=part(a, 2 * px + py), send_sem=sems[0].at[a, k],
                    recv_sem=sems[1].at[a, k], device_id=(px, py, c), device_id_type=MESH)
                out.append((send, recv))
        return out

    def start(r_in, r_out, sems):
        for send, _ in copies(r_out, sems):
            send().start()

    def wait(r_in, r_out, sems):
        cps = copies(r_out, sems)
        for _, recv in cps:
            recv().wait_recv()
        for send, _ in cps:
            send().wait_send()

    ops = list(big) + list(small)
    return _Rider(ops, [jax.ShapeDtypeStruct(o.shape, o.dtype) for o in ops], {i: i for i in range(n)},
                  [pltpu.SemaphoreType.DMA((n, 3)), pltpu.SemaphoreType.DMA((n, 3))], start, wait)


def _gather_forward_rider(big):
    n = len(big)

    def copies(bufs, sems):
        x, y, c = _mesh_pos()
        out = []
        for a in range(n):
            for k, (px, py) in enumerate(_other_chips(x, y)):
                slot = 2 * px + py
                send = functools.partial(
                    pltpu.make_async_remote_copy,
                    src_ref=bufs[a].at[slot, c], dst_ref=bufs[a].at[slot, c], send_sem=sems[0].at[a, k],
                    recv_sem=sems[1].at[a, k], device_id=(x, y, 1 - c), device_id_type=MESH)
                recv = functools.partial(
                    pltpu.make_async_remote_copy,
                    src_ref=bufs[a].at[slot, 1 - c], dst_ref=bufs[a].at[slot, 1 - c], send_sem=sems[0].at[a, k],
                    recv_sem=sems[1].at[a, k], device_id=(x, y, 1 - c), device_id_type=MESH)
                out.append((send, recv))
        return out

    def start(r_in, r_out, sems):
        for send, _ in copies(r_out, sems):
            send().start()

    def wait(r_in, r_out, sems):
        cps = copies(r_out, sems)
        for _, recv in cps:
            recv().wait_recv()
        for send, _ in cps:
            send().wait_send()

    return _Rider(big, [jax.ShapeDtypeStruct(o.shape, o.dtype) for o in big], {i: i for i in range(n)},
                  [pltpu.SemaphoreType.DMA((n, 3)), pltpu.SemaphoreType.DMA((n, 3))], start, wait)


def _pair_exchange_rider(gs, halved):
    n = len(gs)

    def copies(r_in, r_out, sems):
        x, y, c = _mesh_pos()
        return [pltpu.make_async_remote_copy(
            src_ref=r_in[a].at[:, 1 - c] if halved else r_in[a], dst_ref=r_out[a], send_sem=sems[0].at[a],
            recv_sem=sems[1].at[a], device_id=(x, y, 1 - c), device_id_type=MESH) for a in range(n)]

    def start(r_in, r_out, sems):
        for cp in copies(r_in, r_out, sems):
            cp.start()

    def wait(r_in, r_out, sems):
        for cp in copies(r_in, r_out, sems):
            cp.wait()

    return _Rider(gs, [jax.ShapeDtypeStruct((g.shape[0],) + g.shape[-2:], g.dtype) for g in gs], {},
                  [pltpu.SemaphoreType.DMA((n,)), pltpu.SemaphoreType.DMA((n,))], start, wait)


def _chip_exchange_rider(to_send, by_chip, row_range=None):
    n = len(to_send)

    def copies(r_in, r_out, sems):
        x, y, c = _mesh_pos()
        me = 2 * x + y
        rows = (lambda ref: ref) if row_range is None else (lambda ref: ref.at[pl.ds(*row_range)])
        out = []
        for a in range(n):
            for k, (px, py) in enumerate(_other_chips(x, y)):
                send = functools.partial(
                    pltpu.make_async_remote_copy,
                    src_ref=rows(r_in[a].at[2 * px + py]), dst_ref=rows(r_out[a].at[me]),
                    send_sem=sems[0].at[a, k], recv_sem=sems[1].at[a, k], device_id=(px, py, c),
                    device_id_type=MESH)
                recv = functools.partial(
                    pltpu.make_async_remote_copy,
                    src_ref=rows(r_in[a].at[me]), dst_ref=rows(r_out[a].at[2 * px + py]),
                    send_sem=sems[0].at[a, k], recv_sem=sems[1].at[a, k], device_id=(px, py, c),
                    device_id_type=MESH)
                out.append((send, recv))
        return out

    def start(r_in, r_out, sems):
        for send, _ in copies(r_in, r_out, sems):
            send().start()

    def wait(r_in, r_out, sems):
        cps = copies(r_in, r_out, sems)
        for _, recv in cps:
            recv().wait_recv()
        for send, _ in cps:
            send().wait_send()

    return _Rider(list(to_send) + list(by_chip), [jax.ShapeDtypeStruct(b.shape, b.dtype) for b in by_chip],
                  {n + i: i for i in range(n)},
                  [pltpu.SemaphoreType.DMA((n, 3)), pltpu.SemaphoreType.DMA((n, 3))], start, wait)


HBM = pl.BlockSpec(memory_space=pltpu.HBM)
SEM = pl.BlockSpec(memory_space=pltpu.SEMAPHORE)


_IN_FLIGHT = pltpu.CompilerParams(has_side_effects=pltpu.SideEffectType.DATAFLOW_SIDE_EFFECTING)


class _FlatSems:
    def __init__(self, ref, shape):
        self.ref, self.shape = ref, shape

    @property
    def at(self):
        return self

    def __getitem__(self, idx):
        idx = idx if isinstance(idx, tuple) else (idx,)
        flat = 0
        for i, n in zip(idx, self.shape):
            flat = flat * n + i
        return self.ref.at[flat]


def _flat_sem_types(rider):
    return tuple(pltpu.SemaphoreType.DMA((int(np.prod(s.shape)),)) for s in rider.scratch)


def _as_rider_sems(rider, refs):
    return [_FlatSems(r, s.shape) for r, s in zip(refs, rider.scratch)]


def _split_start(rider, name, after=()):
    n_in, n_out, n_sem = len(rider.operands), len(rider.out_shapes), len(rider.scratch)
    n_after = len(after)
    fresh = [j for j in range(n_out) if j not in rider.aliases.values()]
    by_out = {j: i for i, j in rider.aliases.items()}

    def body(*refs):
        r_in = refs[:n_in]
        refs = refs[n_in + n_after:]
        sems = refs[:n_sem]
        thru = refs[n_sem:n_sem + n_in]
        fresh_refs = refs[n_sem + n_in:n_sem + n_in + len(fresh)]
        token = refs[-1]
        r_out = [thru[by_out[j]] if j in by_out else fresh_refs[fresh.index(j)] for j in range(n_out)]
        rider.start(r_in, r_out, _as_rider_sems(rider, sems))
        token[...] = jnp.zeros_like(token)

    res = pl.pallas_call(
        body, name=name,
        out_shape=_flat_sem_types(rider) + tuple(pltpu.HBM(o.shape, o.dtype) for o in rider.operands)
        + tuple(pltpu.HBM(rider.out_shapes[j].shape, rider.out_shapes[j].dtype) for j in fresh)
        + (jax.ShapeDtypeStruct((8, LANES), F32),),
        in_specs=(HBM,) * n_in + (ANY,) * n_after,
        out_specs=(SEM,) * n_sem + (HBM,) * (n_in + len(fresh)) + (pl.BlockSpec(memory_space=pltpu.VMEM),),
        input_output_aliases={i: n_sem + i for i in range(n_in)}, compiler_params=_IN_FLIGHT,
    )(*[pltpu.with_memory_space_constraint(o, pltpu.HBM) for o in rider.operands], *after)
    return (rider, res[:n_sem], res[n_sem:n_sem + n_in], res[n_sem + n_in:-1]), res[-1]


def _split_wait(handles, after, name):
    rider, sems, thru, fresh_arrays = handles
    n_in, n_out, n_sem = len(rider.operands), len(rider.out_shapes), len(rider.scratch)
    fresh = [j for j in range(n_out) if j not in rider.aliases.values()]
    by_out = {j: i for i, j in rider.aliases.items()}
    n_data = n_in + len(fresh)

    def body(*refs):
        r_in = refs[:n_in]
        fresh_refs = refs[n_in:n_data]
        sem_refs = refs[n_data:n_data + n_sem]
        r_out = [r_in[by_out[j]] if j in by_out else fresh_refs[fresh.index(j)] for j in range(n_out)]
        rider.wait(r_in, r_out, _as_rider_sems(rider, sem_refs))

    data = list(thru) + list(fresh_arrays)
    res = pl.pallas_call(
        body, name=name, out_shape=tuple(pltpu.HBM(d.shape, d.dtype) for d in data),
        in_specs=(HBM,) * n_data + (SEM,) * n_sem + (ANY,) * len(after), out_specs=(HBM,) * n_data,
        input_output_aliases={i: i for i in range(n_data)}, compiler_params=_IN_FLIGHT,
    )(*data, *sems, *after)
    return [res[by_out[j]] if j in by_out else res[n_in + fresh.index(j)] for j in range(n_out)], res[:n_in]


def _pair_gather_rider(bufs):
    n = len(bufs)

    def copies(r_out, sems):
        x, y, c = _mesh_pos()
        out = []
        for a in range(n):
            send = functools.partial(
                    pltpu.make_async_remote_copy,
                src_ref=r_out[a].at[c], dst_ref=r_out[a].at[c], send_sem=sems[0].at[a],
                recv_sem=sems[1].at[a], device_id=(x, y, 1 - c), device_id_type=MESH)
            recv = functools.partial(
                    pltpu.make_async_remote_copy,
                src_ref=r_out[a].at[1 - c], dst_ref=r_out[a].at[1 - c], send_sem=sems[0].at[a],
                recv_sem=sems[1].at[a], device_id=(x, y, 1 - c), device_id_type=MESH)
            out.append((send, recv))
        return out

    def start(r_in, r_out, sems):
        for send, _ in copies(r_out, sems):
            send().start()

    def wait(r_in, r_out, sems):
        cps = copies(r_out, sems)
        for _, recv in cps:
            recv().wait_recv()
        for send, _ in cps:
            send().wait_send()

    return _Rider(bufs, [jax.ShapeDtypeStruct(b.shape, b.dtype) for b in bufs], {i: i for i in range(n)},
                  [pltpu.SemaphoreType.DMA((n,)), pltpu.SemaphoreType.DMA((n,))], start, wait)


def _add_own_half(g, recv, pos_arr, name):
    nb, rh, cols = g.shape[0], g.shape[-2], g.shape[-1]

    def body(pos_ref, g_ref, r_ref, send_ref, own_ref):
        s = (g_ref[...] + r_ref[...]).astype(BF16)
        send_ref[...] = s

        @pl.when(pl.program_id(0) == pos_ref[1])
        def _():
            own_ref[...] = s

    blk = pl.BlockSpec((None, rh, cols), lambda j, pos_ref: (j, 0, 0))
    g_spec = blk if g.ndim == 3 else pl.BlockSpec((None, None, rh, cols),
                                                   lambda j, pos_ref: (j, pos_ref[0], 0, 0))
    shape = jax.ShapeDtypeStruct((nb, rh, cols), BF16)
    return pl.pallas_call(
        body, name=name,
        grid_spec=pltpu.PrefetchScalarGridSpec(
            num_scalar_prefetch=1, grid=(nb,), in_specs=[g_spec, blk],
            out_specs=[blk, pl.BlockSpec((None, rh, cols), lambda j, pos_ref: (pos_ref[1], 0, 0))]),
        out_shape=[shape, shape], compiler_params=_params(),
    )(pos_arr, g, recv)


def _sum_chips(gath, pos_arr, name):
    nb, rh, cols = gath.shape

    def body(pos_ref, a_ref, b_ref, c_ref, d_ref, o_ref):
        del pos_ref
        o_ref[...] = ((a_ref[...].astype(F32) + b_ref[...].astype(F32)) + c_ref[...].astype(F32)) \
            + d_ref[...].astype(F32)

    tr = rh // 2 if (rh // 2) % 16 == 0 else rh
    specs = [pl.BlockSpec((None, tr, cols), functools.partial(lambda i, pos_ref, j: (j, i, 0), j=j))
             for j in range(nb)]
    return pl.pallas_call(
        body, name=name,
        grid_spec=pltpu.PrefetchScalarGridSpec(
            num_scalar_prefetch=1, grid=(rh // tr,), in_specs=specs,
            out_specs=pl.BlockSpec((None, tr, cols), lambda i, pos_ref: (pos_ref[0], i, 0))),
        out_shape=jax.ShapeDtypeStruct((2, rh, cols), F32), compiler_params=_params(),
    )(pos_arr, gath, gath, gath, gath)


def _small_allreduce(v, name, rider=None, after=()):
    n = v.shape[0]
    n_dev = 8

    def body(v_ref, o_ref, buf, send_sems, recv_sems):
        x, y, c = _mesh_pos()
        me = 4 * x + 2 * y + c
        buf[me] = v_ref[...]
        sends = []
        peers = []
        for r in range(1, n_dev):
            px = 1 - x if r & 4 else x
            py = 1 - y if r & 2 else y
            pc = 1 - c if r & 1 else c
            peers.append((px, py, pc))
            cp = pltpu.make_async_remote_copy(
                src_ref=v_ref, dst_ref=buf.at[me], send_sem=send_sems.at[r - 1],
                recv_sem=recv_sems.at[r - 1], device_id=(px, py, pc), device_id_type=MESH)
            cp.start()
            sends.append(cp)
        for r, (px, py, pc) in enumerate(peers):
            pltpu.make_async_remote_copy(
                src_ref=v_ref, dst_ref=buf.at[4 * px + 2 * py + pc], send_sem=send_sems.at[r],
                recv_sem=recv_sems.at[r], device_id=(px, py, pc), device_id_type=MESH).wait_recv()
        for cp in sends:
            cp.wait_send()
        acc = buf[0]
        for i in range(1, n_dev):
            acc = acc + buf[i]
        o_ref[...] = acc

    whole = pl.BlockSpec(v.shape, lambda i: (0, 0))
    return _pallas(
        body, name=name, grid=(1,), in_specs=[whole], out_specs=whole,
        out_shape=jax.ShapeDtypeStruct(v.shape, v.dtype), operands=[v],
        scratch_shapes=[pltpu.VMEM((n_dev, n, LANES), F32), pltpu.SemaphoreType.DMA((n_dev - 1,)),
                        pltpu.SemaphoreType.DMA((n_dev - 1,))],
        rider=rider, after=after)


def _adamw_math(w, g, m, v):
    m = ADAM_B1 * m + (1.0 - ADAM_B1) * g
    v = ADAM_B2 * v + (1.0 - ADAM_B2) * (g * g)
    m_hat = m / (1.0 - ADAM_B1 ** ADAM_STEP)
    v_hat = v / (1.0 - ADAM_B2 ** ADAM_STEP)
    delta = -ADAM_LR * (m_hat / (jnp.sqrt(v_hat) + ADAM_EPS) + ADAM_WD * w)
    return delta, m, v


def _adamw(w, g, m, v, name, after=()):
    r, c = w.shape
    tr = 128 if r % 128 == 0 else 64
    assert r % tr == 0

    def body(w_ref, g_ref, m_ref, v_ref, go_ref, d_ref, mo_ref, vo_ref):
        gv = g_ref[...]
        d, mn, vn = _adamw_math(w_ref[...], gv, m_ref[...], v_ref[...])
        go_ref[...] = gv
        d_ref[...] = d
        mo_ref[...] = mn
        vo_ref[...] = vn

    blk = pl.BlockSpec((tr, c), lambda i: (i, 0))
    return _pallas(body, name=name, grid=(r // tr,), in_specs=[blk] * 4, out_specs=[blk] * 4,
                   out_shape=[jax.ShapeDtypeStruct((r, c), F32)] * 4, operands=[w, g, m, v], after=after)


def _adamw_small(ws, gs, ms, vs, name):
    n = len(ws)

    def body(*refs):
        w_r, g_r, m_r, v_r = refs[:n], refs[n:2 * n], refs[2 * n:3 * n], refs[3 * n:4 * n]
        d_o, m_o, v_o = refs[4 * n:5 * n], refs[5 * n:6 * n], refs[6 * n:7 * n]
        for i in range(n):
            d, mn, vn = _adamw_math(w_r[i][...], g_r[i][...], m_r[i][...], v_r[i][...])
            d_o[i][...] = d
            m_o[i][...] = mn
            v_o[i][...] = vn

    specs = [pl.BlockSpec(w.shape, lambda i: (0, 0)) for w in ws]
    shapes = [jax.ShapeDtypeStruct(w.shape, F32) for w in ws]
    outs = pl.pallas_call(
        body, name=name, grid=(1,), in_specs=specs * 4, out_specs=specs * 3, out_shape=shapes * 3,
        compiler_params=_params(),
    )(*ws, *gs, *ms, *vs)
    return outs[:n], outs[n:2 * n], outs[2 * n:]


BIG = ("w_in", "w_o_attn", "w_pw_conv", "w_out", "w_ffn_in", "w_ffn_out")
ROW_SHARDED = ("w_out", "w_ffn_out")
SMALL = ("norm1_w", "b_gate", "q_norm_w", "k_norm_w", "conv_w", "conv_b", "conv_ln_w", "conv_ln_b", "norm2_w")
ORDER = ("norm1_w", "w_in", "b_gate", "q_norm_w", "k_norm_w", "w_o_attn", "conv_w", "conv_b", "conv_ln_w",
         "conv_ln_b", "w_pw_conv", "w_out", "norm2_w", "w_ffn_in", "w_ffn_out")
PACK_TILE = 8 * LANES


def _pack_small(parts):
    rows = []
    for p in parts:
        flat = p.reshape(-1)
        pad = (-flat.shape[0]) % PACK_TILE
        rows.append(jnp.pad(flat, (0, pad)).reshape(-1, LANES))
    return jnp.concatenate(rows, axis=0)


def _unpack_small(packed, shapes):
    out, row = [], 0
    for shp in shapes:
        size = int(np.prod(shp))
        nrow = -(-size // PACK_TILE) * (PACK_TILE // LANES)
        out.append(packed[row:row + nrow].reshape(-1)[:size].reshape(shp))
        row += nrow
    return out


def kernel(x, positions, norm1_w, w_in, b_gate, q_norm_w, k_norm_w, w_o_attn, conv_w, conv_b, conv_ln_w, conv_ln_b, w_pw_conv, w_out, norm2_w, w_ffn_in, w_ffn_out, loss_target, m_norm1_w, m_w_in, m_b_gate, m_q_norm_w, m_k_norm_w, m_w_o_attn, m_conv_w, m_conv_b, m_conv_ln_w, m_conv_ln_b, m_w_pw_conv, m_w_out, m_norm2_w, m_w_ffn_in, m_w_ffn_out, v_norm1_w, v_w_in, v_b_gate, v_q_norm_w, v_k_norm_w, v_w_o_attn, v_conv_w, v_conv_b, v_conv_ln_w, v_conv_ln_b, v_w_pw_conv, v_w_out, v_norm2_w, v_w_ffn_in, v_w_ffn_out):
    w = dict(norm1_w=norm1_w, w_in=w_in, b_gate=b_gate, q_norm_w=q_norm_w, k_norm_w=k_norm_w, w_o_attn=w_o_attn,
             conv_w=conv_w, conv_b=conv_b, conv_ln_w=conv_ln_w, conv_ln_b=conv_ln_b, w_pw_conv=w_pw_conv,
             w_out=w_out, norm2_w=norm2_w, w_ffn_in=w_ffn_in, w_ffn_out=w_ffn_out)
    m = dict(norm1_w=m_norm1_w, w_in=m_w_in, b_gate=m_b_gate, q_norm_w=m_q_norm_w, k_norm_w=m_k_norm_w,
             w_o_attn=m_w_o_attn, conv_w=m_conv_w, conv_b=m_conv_b, conv_ln_w=m_conv_ln_w,
             conv_ln_b=m_conv_ln_b, w_pw_conv=m_w_pw_conv, w_out=m_w_out, norm2_w=m_norm2_w,
             w_ffn_in=m_w_ffn_in, w_ffn_out=m_w_ffn_out)
    v = dict(norm1_w=v_norm1_w, w_in=v_w_in, b_gate=v_b_gate, q_norm_w=v_q_norm_w, k_norm_w=v_k_norm_w,
             w_o_attn=v_w_o_attn, conv_w=v_conv_w, conv_b=v_conv_b, conv_ln_w=v_conv_ln_w,
             conv_ln_b=v_conv_ln_b, w_pw_conv=v_w_pw_conv, w_out=v_w_out, norm2_w=v_norm2_w,
             w_ffn_in=v_w_ffn_in, w_ffn_out=v_w_ffn_out)
    cx, cy, cc = _mesh_pos()
    chip = 2 * cx + cy

    chip_arr = chip.reshape(1).astype(jnp.int32)
    pos_arr = jnp.stack([cc, chip]).astype(jnp.int32)
    bufs = {}
    for n in BIG:
        buf = _cast_into_slot(w[n][0], chip_arr, BF16, f"cast_{n}")
        bufs[n] = buf.reshape(N_CHIPS, 2, buf.shape[1] // 2, buf.shape[2])
    small_bufs = [_cast_into_slot(w[n][0], chip_arr, F32, f"slot_{n}") for n in ("conv_w", "b_gate")]
    w_in_buf, conv_w_buf, b_gate_buf = _comm_call(_gather_both_legs_rider([bufs["w_in"]], small_bufs),
                                                  "allgather_w_in")
    wts = dict(w_in=w_in_buf.reshape(N_CHIPS, -1, w_in_buf.shape[3]),
               conv_w=conv_w_buf.transpose(1, 0, 2).reshape(CONV_WIDTH, -1),
               b_gate=b_gate_buf.transpose(1, 0, 2).reshape(2, 1, -1),
               norm1_w=norm1_w, q_norm_w=q_norm_w, k_norm_w=k_norm_w, conv_b=conv_b, conv_ln_w=conv_ln_w,
               conv_ln_b=conv_ln_b, norm2_w=norm2_w)

    loss, grad_x, g, reduced, w_in_in_flight = _forward_backward(
        x[0], positions.reshape(-1, 1), loss_target[0], wts, [bufs[n] for n in LATE_GATHER], pos_arr)
    grads = {n: b.reshape(-1, b.shape[2]) for n, b in reduced.items()}

    w_in_in_flight, started = w_in_in_flight
    delta, new_m, new_v = {}, {}, {}
    for n in EARLY_REDUCE:
        grads[n], delta[n], new_m[n], new_v[n] = _adamw(w[n][0], grads[n], m[n][0], v[n][0], f"adamw_{n}",
                                                        after=[started])
    small_parts = [loss] + [g[n] for n in SMALL]
    small_shapes = [p.shape for p in small_parts]
    summed = _small_allreduce(_pack_small(small_parts), "small_allreduce",
                              after=[delta[n] for n in EARLY_REDUCE])
    reduced = _unpack_small(summed, small_shapes)
    loss_total = reduced[0].reshape(())
    for n, r in zip(SMALL, reduced[1:]):
        grads[n] = r
    ch_shard = conv_w.shape[2]
    grads["conv_w"] = lax.dynamic_slice_in_dim(grads["conv_w"], chip * ch_shard, ch_shard, axis=1)
    d_shard = b_gate.shape[2]
    grads["b_gate"] = lax.dynamic_slice_in_dim(grads["b_gate"], chip * d_shard, d_shard, axis=1)

    (by_chip_w_in,), _ = _split_wait(w_in_in_flight, after=[delta[n] for n in EARLY_REDUCE] + [summed],
                                     name="grads_w_in_exchange_wait")
    half_w_in = _sum_chips(by_chip_w_in, pos_arr, "grads_chip_sum_w_in")
    (shard_w_in,) = _comm_call(_pair_gather_rider([half_w_in]), "grads_pair_gather_w_in")
    grads["w_in"], delta["w_in"], new_m["w_in"], new_v["w_in"] = _adamw(
        w["w_in"][0], shard_w_in.reshape(-1, shard_w_in.shape[2]), m["w_in"][0], v["w_in"][0], "adamw_w_in")
    flat2 = lambda a: a.reshape(-1, a.shape[-1])
    d_s, m_s, v_s = _adamw_small([flat2(w[n]) for n in SMALL], [flat2(grads[n]) for n in SMALL],
                                 [flat2(m[n]) for n in SMALL], [flat2(v[n]) for n in SMALL], "adamw_small")
    for i, n in enumerate(SMALL):
        delta[n], new_m[n], new_v[n] = d_s[i], m_s[i], v_s[i]

    shaped = lambda d, n: d[n].reshape(w[n].shape)
    return (loss_total, grad_x[None], *[shaped(grads, n) for n in ORDER], *[shaped(delta, n) for n in ORDER],
            *[shaped(new_m, n) for n in ORDER], *[shaped(new_v, n) for n in ORDER])
```

```python
import functools

import numpy as np
import jax
import jax.numpy as jnp
from jax import lax
from jax.experimental import pallas as pl
from jax.experimental.pallas import tpu as pltpu

F32 = jnp.float32
BF16 = jnp.bfloat16
MESH = pl.DeviceIdType.MESH
ANY = pl.BlockSpec(memory_space=pl.ANY)

HEAD_DIM = 64
N_SLOT_HEADS = 8
DILATIONS = (1, 4, 16)
HALF_SPAN = 64
ROPE_THETA = 500000.0
ROT_DIM = 16
CONV_WIDTH = 31
EPS = 1e-6
NEG_INF = -1e30
ADAM_LR, ADAM_B1, ADAM_B2, ADAM_EPS, ADAM_WD, ADAM_STEP = 0.001, 0.9, 0.999, 1e-08, 0.01, 10

LANES = 128
QBLK = 128
KWIN = QBLK + 2 * HALF_SPAN
VMEM_LIMIT = 40 * 1024 * 1024
N_CHIPS = 4


def _params(**kw):
    return pltpu.CompilerParams(vmem_limit_bytes=VMEM_LIMIT, **kw)


class _Rider:
    def __init__(self, operands, out_shapes, aliases, scratch, start, wait):
        self.operands, self.out_shapes, self.aliases = list(operands), list(out_shapes), dict(aliases)
        self.scratch, self.start, self.wait = list(scratch), start, wait


def _pallas(body, *, name, grid, in_specs, out_specs, out_shape, operands, scratch_shapes=(), aliases=None,
            rider=None, after=()):
    single = not isinstance(out_specs, (list, tuple))
    out_specs_l = [out_specs] if single else list(out_specs)
    out_shape_l = [out_shape] if single else list(out_shape)
    aliases = dict(aliases or {})

    def call(fn, all_in_specs, all_out_specs, all_out_shape, all_scratch, all_aliases, all_operands):
        return pl.pallas_call(
            fn, name=name, grid=grid, in_specs=all_in_specs, out_specs=all_out_specs, out_shape=all_out_shape,
            scratch_shapes=all_scratch, input_output_aliases=all_aliases, compiler_params=_params(),
        )(*all_operands)

    if rider is None:
        n_main = len(in_specs)

        def ordered(*refs):
            body(*refs[:n_main], *refs[n_main + len(after):])

        res = call(ordered if after else body, list(in_specs) + [ANY] * len(after), out_specs_l, out_shape_l,
                   list(scratch_shapes), aliases, list(operands) + list(after))
        return res[0] if single else res
    assert not after
    n_in, n_rin = len(in_specs), len(rider.operands)
    n_out, n_rout = len(out_specs_l), len(rider.out_shapes)
    n_sc = len(scratch_shapes)

    def wrapped(*refs):
        main_in, r_in = refs[:n_in], refs[n_in:n_in + n_rin]
        o0 = n_in + n_rin
        main_out, r_out = refs[o0:o0 + n_out], refs[o0 + n_out:o0 + n_out + n_rout]
        s0 = o0 + n_out + n_rout
        main_sc, r_sc = refs[s0:s0 + n_sc], refs[s0 + n_sc:]
        ids = [pl.program_id(d) for d in range(len(grid))]
        first = functools.reduce(jnp.logical_and, [i == 0 for i in ids])
        last = functools.reduce(jnp.logical_and, [i == n - 1 for i, n in zip(ids, grid)])

        @pl.when(first)
        def _():
            rider.start(r_in, r_out, r_sc)

        body(*main_in, *main_out, *main_sc)

        @pl.when(last)
        def _():
            rider.wait(r_in, r_out, r_sc)

    for src, dst in rider.aliases.items():
        aliases[n_in + src] = n_out + dst
    res = call(wrapped, list(in_specs) + [ANY] * n_rin, out_specs_l + [ANY] * n_rout,
               out_shape_l + rider.out_shapes, list(scratch_shapes) + rider.scratch, aliases,
               list(operands) + rider.operands)
    main = res[:n_out]
    return (main[0] if single else main), res[n_out:]


def _matmul(a, b, *, mode, tm, tn, tk, out_dtype, name, b_blocked=False,
            out_blocked=None, rider=None, after=()):
    a_shape = a.shape
    if mode == "nn":
        m_dim, k_dim = a_shape
        n_dim = b.shape[0] * b.shape[2] if b_blocked else b.shape[1]
        rows, cols, red = m_dim, n_dim, k_dim
    elif mode == "nt":
        m_dim, n_dim = a_shape
        k_dim = b.shape[1] if b_blocked else b.shape[0]
        rows, cols, red = m_dim, k_dim, n_dim
    else:
        m_dim, k_dim = a_shape
        n_dim = b.shape[1]
        rows, cols, red = k_dim, n_dim, m_dim
    assert rows % tm == 0 and cols % tn == 0 and red % tk == 0, (name, rows, cols, red)
    ni, nj, nk = rows // tm, cols // tn, red // tk

    if mode == "nn":
        a_spec = pl.BlockSpec((tm, tk), lambda i, j, k: (i, k))
        if b_blocked:
            per = b.shape[2] // tn
            b_spec = pl.BlockSpec((None, tk, tn), lambda i, j, k: (j // per, k, j % per))
        else:
            b_spec = pl.BlockSpec((tk, tn), lambda i, j, k: (k, j))
        dims = (((1,), (0,)), ((), ()))
    elif mode == "nt":
        a_spec = pl.BlockSpec((tm, tk), lambda i, j, k: (i, k))
        if b_blocked:
            per = b.shape[2] // tk
            b_spec = pl.BlockSpec((None, tn, tk), lambda i, j, k: (k // per, j, k % per))
        else:
            b_spec = pl.BlockSpec((tn, tk), lambda i, j, k: (j, k))
        dims = (((1,), (1,)), ((), ()))
    else:
        a_spec = pl.BlockSpec((tk, tm), lambda i, j, k: (k, i))
        b_spec = pl.BlockSpec((tk, tn), lambda i, j, k: (k, j))
        dims = (((0,), (0,)), ((), ()))

    if out_blocked:
        per_o = (cols // out_blocked) // tn
        out_spec = pl.BlockSpec((None, tm, tn), lambda i, j, k: (j // per_o, i, j % per_o))
        out_shape = jax.ShapeDtypeStruct((out_blocked, rows, cols // out_blocked), out_dtype)
    else:
        out_spec = pl.BlockSpec((tm, tn), lambda i, j, k: (i, j))
        out_shape = jax.ShapeDtypeStruct((rows, cols), out_dtype)

    def body(a_ref, b_ref, o_ref, *acc):
        prod = lax.dot_general(a_ref[...], b_ref[...], dims, preferred_element_type=F32)
        if nk == 1:
            o_ref[...] = prod.astype(out_dtype)
        else:
            acc_ref, = acc
            k = pl.program_id(2)

            @pl.when(k == 0)
            def _():
                acc_ref[...] = prod

            @pl.when(k > 0)
            def _():
                acc_ref[...] += prod

            @pl.when(k == nk - 1)
            def _():
                o_ref[...] = acc_ref[...].astype(out_dtype)

    scratch = [pltpu.VMEM((tm, tn), F32)] if nk > 1 else []
    return _pallas(body, name=name, grid=(ni, nj, nk), in_specs=[a_spec, b_spec], out_specs=out_spec,
                   out_shape=out_shape, operands=[a, b], scratch_shapes=scratch, rider=rider, after=after)


def _rmsnorm_fwd(x, w, name):
    s, d = x.shape
    tm = 256

    def body(x_ref, w_ref, o_ref):
        xv = x_ref[...]
        rstd = lax.rsqrt(jnp.mean(xv * xv, axis=-1, keepdims=True) + EPS)
        o_ref[...] = (xv * rstd * w_ref[...]).astype(BF16)

    return pl.pallas_call(
        body, name=name, grid=(s // tm,),
        in_specs=[pl.BlockSpec((tm, d), lambda i: (i, 0)), pl.BlockSpec((1, d), lambda i: (0, 0))],
        out_specs=pl.BlockSpec((tm, d), lambda i: (i, 0)),
        out_shape=jax.ShapeDtypeStruct((s, d), BF16), compiler_params=_params(),
    )(x, w)


def _rmsnorm_bwd(dh, x, w, dres, name, rider=None):
    s, d = x.shape
    tm = 256

    def body(dh_ref, x_ref, w_ref, dres_ref, dx_ref, dxb_ref, dw_ref):
        xv = x_ref[...]
        rstd = lax.rsqrt(jnp.mean(xv * xv, axis=-1, keepdims=True) + EPS)
        xhat = xv * rstd
        dhv = dh_ref[...]
        g = dhv * w_ref[...]
        dx = rstd * (g - xhat * jnp.mean(g * xhat, axis=-1, keepdims=True)) + dres_ref[...]
        dx_ref[...] = dx
        dxb_ref[...] = dx.astype(BF16)
        part = jnp.sum(dhv * xhat, axis=0, keepdims=True)

        @pl.when(pl.program_id(0) == 0)
        def _():
            dw_ref[...] = part

        @pl.when(pl.program_id(0) > 0)
        def _():
            dw_ref[...] += part

    row = pl.BlockSpec((tm, d), lambda i: (i, 0))
    vec = pl.BlockSpec((1, d), lambda i: (0, 0))
    return _pallas(
        body, name=name, grid=(s // tm,), in_specs=[row, row, vec, row], out_specs=[row, row, vec],
        out_shape=[jax.ShapeDtypeStruct((s, d), F32), jax.ShapeDtypeStruct((s, d), BF16),
                   jax.ShapeDtypeStruct((1, d), F32)],
        operands=[dh, x, w, dres], rider=rider)


def _rope_consts():
    lane = np.arange(LANES)
    in_head = lane % HEAD_DIM
    inv_freq = ROPE_THETA ** (-jnp.arange(0, ROT_DIM, 2, dtype=F32) / ROT_DIM)
    invf = jnp.where(jnp.asarray(in_head < ROT_DIM), jnp.tile(inv_freq, LANES // (ROT_DIM // 2)), 0.0)
    m_a = np.where(in_head < ROT_DIM // 2, -1.0, 0.0).astype(np.float32)
    m_b = np.where((in_head >= ROT_DIM // 2) & (in_head < ROT_DIM), 1.0, 0.0).astype(np.float32)
    block_diag = (lane[:, None] // HEAD_DIM == lane[None, :] // HEAD_DIM).astype(np.float32)
    return (invf.reshape(1, LANES).astype(F32), jnp.asarray(m_a).reshape(1, LANES),
            jnp.asarray(m_b).reshape(1, LANES), jnp.asarray(block_diag, dtype=BF16))


def _head_sums(v, bd):
    hi = v.astype(BF16)
    lo = (v - hi.astype(F32)).astype(BF16)
    return jnp.dot(hi, bd, preferred_element_type=F32) + jnp.dot(lo, bd, preferred_element_type=F32)


def _qk_fwd(proj, pos_col, qw2, kw2, consts, name, rider=None):
    s = proj.shape[0]
    width = 3 * N_SLOT_HEADS * HEAD_DIM
    tm = 128
    invf, m_a, m_b, bd = consts
    scale = HEAD_DIM ** -0.5

    def body(q_ref, k_ref, pos_ref, qw_ref, kw_ref, invf_ref, ma_ref, mb_ref, bd_ref, qo_ref, ko_ref):
        ang = pos_ref[...].astype(F32) * invf_ref[...]
        cos = jnp.cos(ang)
        sin = jnp.sin(ang)
        s_a = sin * ma_ref[...]
        s_b = sin * mb_ref[...]
        bdv = bd_ref[...]
        for src, w_ref, dst, sc in ((q_ref, qw_ref, qo_ref, scale), (k_ref, kw_ref, ko_ref, 1.0)):
            for cb in range(width // LANES):
                cols = slice(cb * LANES, (cb + 1) * LANES)
                t = src[:, cols]
                rstd = lax.rsqrt(_head_sums(t * t, bdv) * (1.0 / HEAD_DIM) + EPS)
                y = t * rstd * w_ref[...]
                r = y * cos + pltpu.roll(y, LANES - 8, axis=1) * s_a + pltpu.roll(y, 8, axis=1) * s_b
                dst[:, cols] = r * sc if sc != 1.0 else r

    vec = pl.BlockSpec((1, LANES), lambda i: (0, 0))
    return _pallas(
        body, name=name, grid=(s // tm,),
        in_specs=[pl.BlockSpec((tm, width), lambda i: (i, 0)), pl.BlockSpec((tm, width), lambda i: (i, 1)),
                  pl.BlockSpec((tm, 1), lambda i: (i, 0)), vec, vec, vec, vec, vec,
                  pl.BlockSpec((LANES, LANES), lambda i: (0, 0))],
        out_specs=[pl.BlockSpec((tm, width), lambda i: (i, 0))] * 2,
        out_shape=[jax.ShapeDtypeStruct((s, width), F32)] * 2,
        operands=[proj, proj, pos_col, qw2, kw2, invf, m_a, m_b, bd], rider=rider)


def _qk_bwd(dqn, dkn, dv, da, db, dgl, proj, pos_col, qw2, kw2, consts, name, rider=None):
    s = proj.shape[0]
    width = 3 * N_SLOT_HEADS * HEAD_DIM
    ch = da.shape[1]
    gate_w = dgl.shape[1]
    out_w = 3 * width + 2 * ch + gate_w
    assert out_w == proj.shape[1]
    tm = 128
    invf, m_a, m_b, bd = consts
    scale = HEAD_DIM ** -0.5

    def body(dq_ref, dk_ref, dv_ref, da_ref, db_ref, dgl_ref, q_ref, k_ref, pos_ref, qw_ref, kw_ref,
             invf_ref, ma_ref, mb_ref, bd_ref, out_ref, dqw_ref, dkw_ref):
        ang = pos_ref[...].astype(F32) * invf_ref[...]
        cos = jnp.cos(ang)
        sin = jnp.sin(ang)
        s_a = sin * ma_ref[...]
        s_b = sin * mb_ref[...]
        bdv = bd_ref[...]
        first = pl.program_id(0) == 0
        for src, dsrc, w_ref, col0, dw_ref, sc in ((q_ref, dq_ref, qw_ref, 0, dqw_ref, scale),
                                                   (k_ref, dk_ref, kw_ref, width, dkw_ref, 1.0)):
            dw_acc = jnp.zeros((1, LANES), F32)
            for cb in range(width // LANES):
                cols = slice(cb * LANES, (cb + 1) * LANES)
                t = src[:, cols]
                dr = dsrc[:, cols]
                if sc != 1.0:
                    dr = dr * sc
                dy = dr * cos + pltpu.roll(dr * s_a, 8, axis=1) + pltpu.roll(dr * s_b, LANES - 8, axis=1)
                rstd = lax.rsqrt(_head_sums(t * t, bdv) * (1.0 / HEAD_DIM) + EPS)
                xhat = t * rstd
                g = dy * w_ref[...]
                dt = rstd * (g - xhat * (_head_sums(g * xhat, bdv) * (1.0 / HEAD_DIM)))
                out_ref[:, col0 + cb * LANES: col0 + (cb + 1) * LANES] = dt.astype(BF16)
                dw_acc = dw_acc + jnp.sum(dy * xhat, axis=0, keepdims=True)
            dw_acc = dw_acc + pltpu.roll(dw_acc, HEAD_DIM, axis=1)

            @pl.when(first)
            def _(dw_ref=dw_ref, dw_acc=dw_acc):
                dw_ref[...] = dw_acc

            @pl.when(jnp.logical_not(first))
            def _(dw_ref=dw_ref, dw_acc=dw_acc):
                dw_ref[...] += dw_acc
        out_ref[:, 2 * width: 3 * width] = dv_ref[...].astype(BF16)
        out_ref[:, 3 * width: 3 * width + ch] = da_ref[...]
        out_ref[:, 3 * width + ch: 3 * width + 2 * ch] = db_ref[...]
        out_ref[:, 3 * width + 2 * ch: out_w] = dgl_ref[...]

    vec = pl.BlockSpec((1, LANES), lambda i: (0, 0))
    blk = lambda c: pl.BlockSpec((tm, width), lambda i: (i, c))
    cblk = pl.BlockSpec((tm, ch), lambda i: (i, 0))
    return _pallas(
        body, name=name, grid=(s // tm,),
        in_specs=[blk(0), blk(0), blk(0), cblk, cblk, pl.BlockSpec((tm, gate_w), lambda i: (i, 0)),
                  blk(0), blk(1), pl.BlockSpec((tm, 1), lambda i: (i, 0)), vec, vec, vec, vec, vec,
                  pl.BlockSpec((LANES, LANES), lambda i: (0, 0))],
        out_specs=[pl.BlockSpec((tm, out_w), lambda i: (i, 0)), vec, vec],
        out_shape=[jax.ShapeDtypeStruct((s, out_w), BF16)] + [jax.ShapeDtypeStruct((1, LANES), F32)] * 2,
        operands=[dqn, dkn, dv, da, db, dgl, proj, proj, pos_col, qw2, kw2, invf, m_a, m_b, bd],
        rider=rider)


def _row_chunks(n_rows, fn, chunk=256):
    def step(i, c):
        fn(pl.ds(pl.multiple_of(i * chunk, chunk), chunk))
        return c
    lax.fori_loop(0, n_rows // chunk, step, 0)


def _to_residue_major(dst, src, s, d, dst_off=0, cast=None):
    seq = s // d
    for r in range(d):
        v = src[...] if d == 1 else src[pl.ds(r, seq, stride=d), :]
        dst[dst_off + r * seq: dst_off + (r + 1) * seq, :] = v if cast is None else v.astype(cast)


def _from_residue_major(dst, src, s, d, src_off=0):
    seq = s // d
    for r in range(d):
        v = src[src_off + r * seq: src_off + (r + 1) * seq, :]
        if d == 1:
            dst[...] = v
        else:
            dst[pl.ds(r, seq, stride=d), :] = v


def _band_bias():
    qi = lax.broadcasted_iota(jnp.int32, (QBLK, KWIN), 0)
    kj = lax.broadcasted_iota(jnp.int32, (QBLK, KWIN), 1)
    return jnp.where(jnp.abs(kj - HALF_SPAN - qi) <= HALF_SPAN, 0.0, NEG_INF).astype(F32)


def _range_bias(base, seq):
    kj = lax.broadcasted_iota(jnp.int32, (1, KWIN), 1)
    lo = (base & -seq) - base + HALF_SPAN
    return jnp.where((kj >= lo) & (kj < lo + seq), 0.0, NEG_INF).astype(F32)


ATTN_BLOCKS_PER_TRIP = 4


def _skewed_blocks(n_blk, produce, consume):
    per = ATTN_BLOCKS_PER_TRIP
    produce(0, 0)

    def trip(i, carry):
        for u in range(per):
            consume(per * i + u, u % 2)
            produce(per * i + u + 1, (u + 1) % 2)
        return carry

    n_trips = n_blk // per - 1
    lax.fori_loop(0, n_trips, trip, 0)
    for b in range(per * n_trips, n_blk):
        consume(b, b % 2)
        if b + 1 < n_blk:
            produce(b + 1, (b + 1) % 2)


def _block_base(b):
    return b * QBLK if isinstance(b, int) else pl.multiple_of(b * QBLK, QBLK)


def _attn_fwd(qn, kn, proj, name, rider=None):
    s = qn.shape[0]
    n_pairs = N_SLOT_HEADS * HEAD_DIM // LANES
    v_col0 = 2 * qn.shape[1] // LANES
    nt_dims = (((1,), (1,)), ((), ()))

    def body(q_ref, k_ref, v_ref, attn_ref, lse_ref, attn_b_ref, q_rm, k_rm, v_rm, acc_rm, m_rm, l_rm,
             acc_p, m_p, l_p, m_run, l_run, acc_run, band, s_buf, m_buf):
        g = pl.program_id(1)
        zpad = jnp.zeros((HALF_SPAN, LANES), BF16)
        k_rm[0:HALF_SPAN, :] = zpad
        k_rm[s + HALF_SPAN: s + 2 * HALF_SPAN, :] = zpad
        v_rm[0:HALF_SPAN, 0:LANES] = zpad
        v_rm[s + HALF_SPAN: s + 2 * HALF_SPAN, 0:LANES] = zpad

        def ones_rows(rows):
            v_rm[pl.ds(rows.start, rows.size), LANES:2 * LANES] = jnp.ones((rows.size, LANES), BF16)

        _row_chunks(s + 2 * HALF_SPAN, ones_rows, chunk=2 * HALF_SPAN)
        band[...] = _band_bias()
        lane = lax.broadcasted_iota(jnp.int32, (QBLK, LANES), 1)
        low = lane < HEAD_DIM
        n_blk = s // QBLK

        for gi, d in enumerate(DILATIONS):
            @pl.when(g == gi)
            def _(gi=gi, d=d):
                seq = s // d
                _to_residue_major(q_rm, q_ref, s, d, cast=BF16)
                _to_residue_major(k_rm, k_ref, s, d, dst_off=HALF_SPAN, cast=BF16)
                _to_residue_major(v_rm.at[:, 0:LANES], v_ref, s, d, dst_off=HALF_SPAN, cast=BF16)

                def scores(b, slot):
                    base = _block_base(b)
                    q = q_rm[pl.ds(base, QBLK), :]
                    zero = jnp.zeros_like(q)
                    q2 = jnp.concatenate([jnp.where(low, q, zero), jnp.where(low, zero, q)], axis=0)
                    sc = lax.dot_general(q2, k_rm[pl.ds(base, KWIN), :], nt_dims, preferred_element_type=F32)
                    bias = band[...] + _range_bias(base, seq)
                    for hh in range(2):
                        rows = slice(hh * QBLK, (hh + 1) * QBLK)
                        sh = sc[rows, :] + bias
                        s_buf[slot, rows, :] = sh
                        m_buf[slot, rows, :] = jnp.broadcast_to(jnp.max(sh, axis=-1, keepdims=True), (QBLK, LANES))

                def outputs(b, slot):
                    base = _block_base(b)
                    sv = s_buf[slot]
                    mb = m_buf[slot]
                    p = jnp.exp(jnp.concatenate([sv[:, 0:LANES] - mb, sv[:, LANES:2 * LANES] - mb], axis=1))
                    pv = jnp.dot(p.astype(BF16), v_rm[pl.ds(base, KWIN), :], preferred_element_type=F32)
                    rows = pl.ds(base, QBLK)
                    acc_rm[rows, :] = jnp.where(low, pv[0:QBLK, 0:LANES], pv[QBLK:2 * QBLK, 0:LANES])
                    l_rm[rows, :] = jnp.where(low, pv[0:QBLK, LANES:2 * LANES], pv[QBLK:2 * QBLK, LANES:2 * LANES])
                    m_rm[rows, :] = jnp.where(low, mb[0:QBLK, :], mb[QBLK:2 * QBLK, :])

                _skewed_blocks(n_blk, scores, outputs)
                if d == 1:
                    src = (acc_rm, m_rm, l_rm)
                else:
                    for dst_, src_ in ((acc_p, acc_rm), (m_p, m_rm), (l_p, l_rm)):
                        _from_residue_major(dst_, src_, s, d)
                    src = (acc_p, m_p, l_p)

                def combine(rows):
                    a_g, m_g, l_g = src[0][rows, :], src[1][rows, :], src[2][rows, :]
                    if gi == 0:
                        m_new, l_new, a_new = m_g, l_g, a_g
                    else:
                        m_old = m_run[rows, :]
                        m_new = jnp.maximum(m_old, m_g)
                        w_old = jnp.exp(m_old - m_new)
                        w_g = jnp.exp(m_g - m_new)
                        l_new = l_run[rows, :] * w_old + l_g * w_g
                        a_new = acc_run[rows, :] * w_old + a_g * w_g
                    if gi == len(DILATIONS) - 1:
                        out = a_new / l_new
                        attn_ref[rows, :] = out
                        attn_b_ref[rows, :] = out.astype(BF16)
                        lse_ref[rows, :] = m_new + jnp.log(l_new)
                    else:
                        m_run[rows, :] = m_new
                        l_run[rows, :] = l_new
                        acc_run[rows, :] = a_new

                _row_chunks(s, combine)

    qk_spec = pl.BlockSpec((s, LANES), lambda hp, g: (0, g * n_pairs + hp))
    v_spec = pl.BlockSpec((s, LANES), lambda hp, g: (0, v_col0 + g * n_pairs + hp))
    o_spec = pl.BlockSpec((s, LANES), lambda hp, g: (0, hp))
    f32buf = pltpu.VMEM((s, LANES), F32)
    return _pallas(
        body, name=name, grid=(n_pairs, len(DILATIONS)), in_specs=[qk_spec, qk_spec, v_spec],
        out_specs=[o_spec, o_spec, o_spec],
        out_shape=[jax.ShapeDtypeStruct((s, n_pairs * LANES), F32)] * 2
        + [jax.ShapeDtypeStruct((s, n_pairs * LANES), BF16)],
        operands=[qn, kn, proj],
        scratch_shapes=[pltpu.VMEM((s, LANES), BF16), pltpu.VMEM((s + 2 * HALF_SPAN, LANES), BF16),
                        pltpu.VMEM((s + 2 * HALF_SPAN, 2 * LANES), BF16)] + [f32buf] * 9
        + [pltpu.VMEM((QBLK, KWIN), F32), pltpu.VMEM((2, 2 * QBLK, KWIN), F32),
           pltpu.VMEM((2, 2 * QBLK, LANES), F32)],
        rider=rider)


def _attn_bwd(qn, kn, proj, dattn, attn, lse, bd, name, rider=None):
    s = qn.shape[0]
    n_pairs = N_SLOT_HEADS * HEAD_DIM // LANES
    v_col0 = 2 * qn.shape[1] // LANES
    nt_dims = (((1,), (1,)), ((), ()))
    tn_dims = (((0,), (0,)), ((), ()))
    spad = s + 2 * HALF_SPAN

    def body(q_ref, k_ref, v_ref, do_ref, o_ref, lse_ref, bd_ref, dq_ref, dk_ref, dv_ref,
             q_rm, k_rm, v_rm, do_rm, lse0_rm, lse1_rm, dd0_rm, dd1_rm, dq_rm, dk_rm, dv_rm,
             lse0_p, lse1_p, dd0_p, dd1_p, band, p_buf, ds_buf):
        g = pl.program_id(1)
        zpad = jnp.zeros((HALF_SPAN, LANES), BF16)
        for buf in (k_rm, v_rm):
            buf[0:HALF_SPAN, :] = zpad
            buf[s + HALF_SPAN: spad, :] = zpad
        zf = jnp.zeros((HALF_SPAN, LANES), F32)
        for buf in (dk_rm, dv_rm):
            buf[0:HALF_SPAN, :] = zf
            buf[s + HALF_SPAN: spad, :] = zf
        band[...] = _band_bias()

        def clear(rows):
            z = jnp.zeros((rows.size, LANES), F32)
            dk_rm[pl.ds(rows.start + HALF_SPAN, rows.size), :] = z
            dv_rm[pl.ds(rows.start + HALF_SPAN, rows.size), :] = z

        _row_chunks(s, clear)

        def prepare(rows):
            lo = lax.broadcasted_iota(jnp.int32, (rows.size, LANES), 1) < HEAD_DIM
            dsum = _head_sums(do_ref[rows, :] * o_ref[rows, :], bd_ref[...])
            dswap = pltpu.roll(dsum, HEAD_DIM, axis=1)
            dd0_p[rows, :] = jnp.where(lo, dsum, dswap)
            dd1_p[rows, :] = jnp.where(lo, dswap, dsum)
            lv = lse_ref[rows, :]
            lswap = pltpu.roll(lv, HEAD_DIM, axis=1)
            lse0_p[rows, :] = jnp.where(lo, lv, lswap)
            lse1_p[rows, :] = jnp.where(lo, lswap, lv)

        @pl.when(g == 0)
        def _():
            _row_chunks(s, prepare)
        lane = lax.broadcasted_iota(jnp.int32, (QBLK, LANES), 1)
        low = lane < HEAD_DIM
        n_blk = s // QBLK

        def stacked(ref, rows):
            val = ref[rows, :]
            zero = jnp.zeros_like(val)
            return jnp.concatenate([jnp.where(low, val, zero), jnp.where(low, zero, val)], axis=0)

        for gi, d in enumerate(DILATIONS):
            @pl.when(g == gi)
            def _(d=d):
                seq = s // d
                _to_residue_major(q_rm, q_ref, s, d, cast=BF16)
                _to_residue_major(k_rm, k_ref, s, d, dst_off=HALF_SPAN, cast=BF16)
                _to_residue_major(v_rm, v_ref, s, d, dst_off=HALF_SPAN, cast=BF16)
                _to_residue_major(do_rm, do_ref, s, d, cast=BF16)
                for dst_, src_ in ((lse0_rm, lse0_p), (lse1_rm, lse1_p), (dd0_rm, dd0_p), (dd1_rm, dd1_p)):
                    _to_residue_major(dst_, src_, s, d)

                def scores(b, slot):
                    base = _block_base(b)
                    rows = pl.ds(base, QBLK)
                    win = pl.ds(base, KWIN)
                    sc = lax.dot_general(stacked(q_rm, rows), k_rm[win, :], nt_dims, preferred_element_type=F32)
                    dp = lax.dot_general(stacked(do_rm, rows), v_rm[win, :], nt_dims, preferred_element_type=F32)
                    bias = band[...] + _range_bias(base, seq)
                    for hh, (lse_r, dd_r) in enumerate(((lse0_rm, dd0_rm), (lse1_rm, dd1_rm))):
                        r = slice(hh * QBLK, (hh + 1) * QBLK)
                        lse_h = lse_r[rows, :]
                        dd_h = dd_r[rows, :]
                        sh = sc[r, :] + bias
                        p = jnp.exp(jnp.concatenate([sh[:, 0:LANES] - lse_h, sh[:, LANES:KWIN] - lse_h], axis=1))
                        dph = dp[r, :]
                        ds = p * jnp.concatenate([dph[:, 0:LANES] - dd_h, dph[:, LANES:KWIN] - dd_h], axis=1)
                        p_buf[slot, r, :] = p.astype(BF16)
                        ds_buf[slot, r, :] = ds.astype(BF16)

                def grads(b, slot):
                    base = _block_base(b)
                    rows = pl.ds(base, QBLK)
                    win = pl.ds(base, KWIN)
                    p = p_buf[slot]
                    ds = ds_buf[slot]
                    dq2 = jnp.dot(ds, k_rm[win, :], preferred_element_type=F32)
                    dq_rm[rows, :] = jnp.where(low, dq2[0:QBLK, :], dq2[QBLK:2 * QBLK, :])
                    dk_rm[win, :] += lax.dot_general(ds, stacked(q_rm, rows), tn_dims, preferred_element_type=F32)
                    dv_rm[win, :] += lax.dot_general(p, stacked(do_rm, rows), tn_dims, preferred_element_type=F32)

                _skewed_blocks(n_blk, scores, grads)
                _from_residue_major(dq_ref, dq_rm, s, d)
                _from_residue_major(dk_ref, dk_rm, s, d, src_off=HALF_SPAN)
                _from_residue_major(dv_ref, dv_rm, s, d, src_off=HALF_SPAN)

    qk_spec = pl.BlockSpec((s, LANES), lambda hp, g: (0, g * n_pairs + hp))
    v_spec = pl.BlockSpec((s, LANES), lambda hp, g: (0, v_col0 + g * n_pairs + hp))
    o_spec = pl.BlockSpec((s, LANES), lambda hp, g: (0, hp))
    width = qn.shape[1]
    f32buf = pltpu.VMEM((s, LANES), F32)
    f32pad = pltpu.VMEM((spad, LANES), F32)
    return _pallas(
        body, name=name, grid=(n_pairs, len(DILATIONS)),
        in_specs=[qk_spec, qk_spec, v_spec, o_spec, o_spec, o_spec,
                  pl.BlockSpec((LANES, LANES), lambda hp, g: (0, 0))],
        out_specs=[qk_spec, qk_spec, qk_spec],
        out_shape=[jax.ShapeDtypeStruct((s, width), F32)] * 3,
        operands=[qn, kn, proj, dattn, attn, lse, bd],
        scratch_shapes=[pltpu.VMEM((s, LANES), BF16), pltpu.VMEM((spad, LANES), BF16),
                        pltpu.VMEM((spad, LANES), BF16), pltpu.VMEM((s, LANES), BF16),
                        f32buf, f32buf, f32buf, f32buf, f32buf, f32pad, f32pad,
                        f32buf, f32buf, f32buf, f32buf, pltpu.VMEM((QBLK, KWIN), F32),
                        pltpu.VMEM((2, 2 * QBLK, KWIN), BF16), pltpu.VMEM((2, 2 * QBLK, KWIN), BF16)],
        rider=rider)


CONV_PAD = 16


def _conv_fwd(proj, conv_w, conv_b, col0, name, rider=None):
    s = proj.shape[0]
    ch = conv_w.shape[1]
    nblk = ch // LANES
    a0 = col0 // LANES
    tr = 256
    shift = CONV_PAD - (CONV_WIDTH - 1) // 2

    def body(a_ref, b_ref, w_ref, bias_ref, u0_ref, uc_ref, pad):
        z = jnp.zeros((CONV_PAD, LANES), F32)
        pad[0:CONV_PAD, :] = z
        pad[s + CONV_PAD: s + 2 * CONV_PAD, :] = z

        def glu(rows):
            u0 = a_ref[rows, :] * jax.nn.sigmoid(b_ref[rows, :])
            u0_ref[rows, :] = u0
            pad[pl.ds(rows.start + CONV_PAD, rows.size), :] = u0

        _row_chunks(s, glu)
        for t in range(0, s, tr):
            acc = jnp.broadcast_to(bias_ref[...], (tr, LANES))
            for k in range(CONV_WIDTH):
                acc = acc + w_ref[k:k + 1, :] * pad[t + k + shift: t + k + shift + tr, :]
            uc_ref[t:t + tr, :] = acc

    return _pallas(
        body, name=name, grid=(nblk,),
        in_specs=[pl.BlockSpec((s, LANES), lambda c: (0, a0 + c)),
                  pl.BlockSpec((s, LANES), lambda c: (0, a0 + nblk + c)),
                  pl.BlockSpec((CONV_WIDTH, LANES), lambda c: (0, c)),
                  pl.BlockSpec((1, LANES), lambda c: (0, c))],
        out_specs=[pl.BlockSpec((s, LANES), lambda c: (0, c))] * 2,
        out_shape=[jax.ShapeDtypeStruct((s, ch), F32)] * 2, operands=[proj, proj, conv_w, conv_b],
        scratch_shapes=[pltpu.VMEM((s + 2 * CONV_PAD, LANES), F32)], rider=rider)


def _ln_silu_fwd(uc, ln_w, ln_b, name):
    s, ch = uc.shape
    tm = 256

    def body(u_ref, w_ref, b_ref, o_ref):
        u = u_ref[...]
        mu = jnp.mean(u, axis=-1, keepdims=True)
        xc = u - mu
        rstd = lax.rsqrt(jnp.mean(xc * xc, axis=-1, keepdims=True) + EPS)
        z = xc * rstd * w_ref[...] + b_ref[...]
        o_ref[...] = (z * jax.nn.sigmoid(z)).astype(BF16)

    row = pl.BlockSpec((tm, ch), lambda i: (i, 0))
    vec = pl.BlockSpec((1, ch), lambda i: (0, 0))
    return pl.pallas_call(
        body, name=name, grid=(s // tm,), in_specs=[row, vec, vec], out_specs=row,
        out_shape=jax.ShapeDtypeStruct((s, ch), BF16), compiler_params=_params(),
    )(uc, ln_w, ln_b)


def _ln_silu_bwd(du3, uc, ln_w, ln_b, name):
    s, ch = uc.shape
    tm = 256

    def body(d_ref, u_ref, w_ref, b_ref, du_ref, dw_ref, db_ref):
        u = u_ref[...]
        mu = jnp.mean(u, axis=-1, keepdims=True)
        xc = u - mu
        rstd = lax.rsqrt(jnp.mean(xc * xc, axis=-1, keepdims=True) + EPS)
        xhat = xc * rstd
        z = xhat * w_ref[...] + b_ref[...]
        sg = jax.nn.sigmoid(z)
        dz = d_ref[...] * (sg * (1.0 + z * (1.0 - sg)))
        dxh = dz * w_ref[...]
        du_ref[...] = rstd * (dxh - jnp.mean(dxh, axis=-1, keepdims=True)
                              - xhat * jnp.mean(dxh * xhat, axis=-1, keepdims=True))
        pw = jnp.sum(dz * xhat, axis=0, keepdims=True)
        pb = jnp.sum(dz, axis=0, keepdims=True)
        first = pl.program_id(0) == 0

        @pl.when(first)
        def _():
            dw_ref[...] = pw
            db_ref[...] = pb

        @pl.when(jnp.logical_not(first))
        def _():
            dw_ref[...] += pw
            db_ref[...] += pb

    row = pl.BlockSpec((tm, ch), lambda i: (i, 0))
    vec = pl.BlockSpec((1, ch), lambda i: (0, 0))
    return pl.pallas_call(
        body, name=name, grid=(s // tm,), in_specs=[row, row, vec, vec], out_specs=[row, vec, vec],
        out_shape=[jax.ShapeDtypeStruct((s, ch), F32), jax.ShapeDtypeStruct((1, ch), F32),
                   jax.ShapeDtypeStruct((1, ch), F32)],
        compiler_params=_params(),
    )(du3, uc, ln_w, ln_b)


def _conv_bwd(duc, u0, proj, conv_w, col0, name, rider=None):
    s = proj.shape[0]
    ch = conv_w.shape[1]
    nblk = ch // LANES
    a0 = col0 // LANES
    tr = 256
    half = (CONV_WIDTH - 1) // 2
    shift = CONV_PAD - half

    def body(duc_ref, u0_ref, a_ref, b_ref, w_ref, da_ref, db_ref, dw_ref, dbias_ref, pad_d, pad_u):
        z = jnp.zeros((CONV_PAD, LANES), F32)
        for buf in (pad_d, pad_u):
            buf[0:CONV_PAD, :] = z
            buf[s + CONV_PAD: s + 2 * CONV_PAD, :] = z

        def fill(rows):
            dst = pl.ds(rows.start + CONV_PAD, rows.size)
            pad_d[dst, :] = duc_ref[rows, :]
            pad_u[dst, :] = u0_ref[rows, :]

        _row_chunks(s, fill)
        dw_acc = [jnp.zeros((8, LANES), F32) for _ in range(CONV_WIDTH)]
        dbias_acc = jnp.zeros((8, LANES), F32)
        for t in range(0, s, tr):
            d_t = duc_ref[t:t + tr, :]
            dbias_acc = dbias_acc + jnp.sum(d_t.reshape(tr // 8, 8, LANES), axis=0)
            du0 = jnp.zeros((tr, LANES), F32)
            for k in range(CONV_WIDTH):
                du0 = du0 + w_ref[k:k + 1, :] * pad_d[t - k + half + CONV_PAD: t - k + half + CONV_PAD + tr, :]
                prod = d_t * pad_u[t + k + shift: t + k + shift + tr, :]
                dw_acc[k] = dw_acc[k] + jnp.sum(prod.reshape(tr // 8, 8, LANES), axis=0)
            av = a_ref[t:t + tr, :]
            sg = jax.nn.sigmoid(b_ref[t:t + tr, :])
            da_ref[t:t + tr, :] = (du0 * sg).astype(BF16)
            db_ref[t:t + tr, :] = (du0 * av * sg * (1.0 - sg)).astype(BF16)
        for k in range(CONV_WIDTH):
            dw_ref[k:k + 1, :] = jnp.sum(dw_acc[k], axis=0, keepdims=True)
        dbias_ref[...] = jnp.sum(dbias_acc, axis=0, keepdims=True)

    col = lambda off: pl.BlockSpec((s, LANES), lambda c: (0, off + c))
    return _pallas(
        body, name=name, grid=(nblk,),
        in_specs=[col(0), col(0), col(a0), col(a0 + nblk),
                  pl.BlockSpec((CONV_WIDTH, LANES), lambda c: (0, c))],
        out_specs=[col(0), col(0), pl.BlockSpec((CONV_WIDTH, LANES), lambda c: (0, c)),
                   pl.BlockSpec((1, LANES), lambda c: (0, c))],
        out_shape=[jax.ShapeDtypeStruct((s, ch), BF16)] * 2
        + [jax.ShapeDtypeStruct((CONV_WIDTH, ch), F32), jax.ShapeDtypeStruct((1, ch), F32)],
        operands=[duc, u0, proj, proj, conv_w],
        scratch_shapes=[pltpu.VMEM((s + 2 * CONV_PAD, LANES), F32)] * 2, rider=rider)


GATE_BLK = 512


def _gate_fwd(proj, bg, y_a, y_b, col0, name):
    s, d = y_a.shape
    tm = 256
    g0 = col0 // GATE_BLK
    nb = d // GATE_BLK

    def body(ga_ref, gb_ref, ba_ref, bb_ref, ya_ref, yb_ref, o_ref):
        g_a = jax.nn.sigmoid(ga_ref[...] + ba_ref[...])
        g_b = jax.nn.sigmoid(gb_ref[...] + bb_ref[...])
        o_ref[...] = (g_a * ya_ref[...] + g_b * yb_ref[...]).astype(BF16)

    act = pl.BlockSpec((tm, GATE_BLK), lambda i, j: (i, j))
    return pl.pallas_call(
        body, name=name, grid=(s // tm, nb),
        in_specs=[pl.BlockSpec((tm, GATE_BLK), lambda i, j: (i, g0 + j)),
                  pl.BlockSpec((tm, GATE_BLK), lambda i, j: (i, g0 + nb + j)),
                  pl.BlockSpec((None, 1, GATE_BLK), lambda i, j: (0, 0, j)),
                  pl.BlockSpec((None, 1, GATE_BLK), lambda i, j: (1, 0, j)), act, act],
        out_specs=act, out_shape=jax.ShapeDtypeStruct((s, d), BF16), compiler_params=_params(),
    )(proj, proj, bg, bg, y_a, y_b)


def _gate_bwd(d_mixed, proj, bg, y_a, y_b, col0, name, rider=None):
    s, d = y_a.shape
    tm = 256
    half = d // 2
    assert col0 % half == 0
    c0 = col0 // half

    def body(dm_ref, a0_ref, a1_ref, b0_ref, b1_ref, bias_ref, ya_ref, yb_ref, dgl_ref, dya_ref, dyb_ref, db_ref):
        dm = dm_ref[...]
        parts = []
        for br, (lo_ref, hi_ref, y_ref, dy_ref) in enumerate(((a0_ref, a1_ref, ya_ref, dya_ref),
                                                              (b0_ref, b1_ref, yb_ref, dyb_ref))):
            logits = jnp.concatenate([lo_ref[...], hi_ref[...]], axis=1)
            gate = jax.nn.sigmoid(logits + bias_ref[br])
            dy_ref[...] = (dm * gate).astype(BF16)
            dgl = dm * y_ref[...] * gate * (1.0 - gate)
            dgl_ref[:, br * d:(br + 1) * d] = dgl.astype(BF16)
            parts.append(jnp.sum(dgl, axis=0, keepdims=True))
        part = jnp.concatenate(parts, axis=0)
        first = pl.program_id(0) == 0

        @pl.when(first)
        def _():
            db_ref[...] = part

        @pl.when(jnp.logical_not(first))
        def _():
            db_ref[...] += part

    row = pl.BlockSpec((tm, d), lambda i: (i, 0))
    logit_blk = lambda k: pl.BlockSpec((tm, half), functools.partial(lambda i, k: (i, c0 + k), k=k))
    return _pallas(
        body, name=name, grid=(s // tm,),
        in_specs=[row, logit_blk(0), logit_blk(1), logit_blk(2), logit_blk(3),
                  pl.BlockSpec((2, 1, d), lambda i: (0, 0, 0)), row, row],
        out_specs=[pl.BlockSpec((tm, 2 * d), lambda i: (i, 0)), row, row, pl.BlockSpec((2, d), lambda i: (0, 0))],
        out_shape=[jax.ShapeDtypeStruct((s, 2 * d), BF16), jax.ShapeDtypeStruct((s, d), BF16),
                   jax.ShapeDtypeStruct((s, d), BF16), jax.ShapeDtypeStruct((2, d), F32)],
        operands=[d_mixed, proj, proj, proj, proj, bg, y_a, y_b], rider=rider)


def _ffn_in_swiglu(h2, w_blocked, name):
    s, k = h2.shape
    nblk, _, tn = w_blocked.shape
    ff = nblk // 2 * tn
    tm = 512

    def body(a_ref, wg_ref, wu_ref, g_ref, u_ref, act_ref):
        a = a_ref[...]
        gt = jnp.dot(a, wg_ref[...], preferred_element_type=F32)
        up = jnp.dot(a, wu_ref[...], preferred_element_type=F32)
        g_ref[...] = gt
        u_ref[...] = up
        act_ref[...] = (gt * jax.nn.sigmoid(gt) * up).astype(BF16)

    out = pl.BlockSpec((tm, tn), lambda i, j: (i, j))
    return pl.pallas_call(
        body, name=name, grid=(s // tm, nblk // 2),
        in_specs=[pl.BlockSpec((tm, k), lambda i, j: (i, 0)),
                  pl.BlockSpec((None, k, tn), lambda i, j: (j, 0, 0)),
                  pl.BlockSpec((None, k, tn), lambda i, j: (nblk // 2 + j, 0, 0))],
        out_specs=[out, out, out],
        out_shape=[jax.ShapeDtypeStruct((s, ff), F32), jax.ShapeDtypeStruct((s, ff), F32),
                   jax.ShapeDtypeStruct((s, ff), BF16)],
        compiler_params=_params(),
    )(h2, w_blocked, w_blocked)


def _swiglu_bwd(gate, up, d_act, name, rider=None):
    s, ff = gate.shape
    tm = 256

    def body(g_ref, u_ref, d_ref, o_ref):
        gt = g_ref[...]
        sg = jax.nn.sigmoid(gt)
        dv = d_ref[...]
        o_ref[:, 0:ff] = (dv * u_ref[...] * (sg * (1.0 + gt * (1.0 - sg)))).astype(BF16)
        o_ref[:, ff:2 * ff] = (dv * gt * sg).astype(BF16)

    row = pl.BlockSpec((tm, ff), lambda i: (i, 0))
    return _pallas(
        body, name=name, grid=(s // tm,), in_specs=[row, row, row],
        out_specs=pl.BlockSpec((tm, 2 * ff), lambda i: (i, 0)),
        out_shape=jax.ShapeDtypeStruct((s, 2 * ff), BF16), operands=[gate, up, d_act], rider=rider)


def _out_proj_rmsnorm(mixed, w_out, x, norm_w, name):
    s, k = mixed.shape
    d = w_out.shape[1]
    tm = 512

    def body(a_ref, w_ref, x_ref, nw_ref, x1_ref, h2_ref):
        x1 = x_ref[...] + jnp.dot(a_ref[...], w_ref[...], preferred_element_type=F32)
        x1_ref[...] = x1
        rstd = lax.rsqrt(jnp.mean(x1 * x1, axis=-1, keepdims=True) + EPS)
        h2_ref[...] = (x1 * rstd * nw_ref[...]).astype(BF16)

    row = pl.BlockSpec((tm, d), lambda i: (i, 0))
    return pl.pallas_call(
        body, name=name, grid=(s // tm,),
        in_specs=[pl.BlockSpec((tm, k), lambda i: (i, 0)), pl.BlockSpec((k, d), lambda i: (0, 0)), row,
                  pl.BlockSpec((1, d), lambda i: (0, 0))],
        out_specs=[row, row],
        out_shape=[jax.ShapeDtypeStruct((s, d), F32), jax.ShapeDtypeStruct((s, d), BF16)],
        compiler_params=_params(),
    )(mixed, w_out, x, norm_w)


def _ffn_out_loss(act, w_ffn_out, x1, target, name):
    s, k = act.shape
    d = w_ffn_out.shape[1]
    tm = 512

    def body(a_ref, w_ref, x1_ref, t_ref, dy_ref, dyb_ref, loss_ref, acc):
        y = x1_ref[...] + jnp.dot(a_ref[...], w_ref[...], preferred_element_type=F32)
        diff = y - t_ref[...]
        dy = diff * (1.0 / d)
        dy_ref[...] = dy
        dyb_ref[...] = dy.astype(BF16)
        part = jnp.sum((diff * diff).reshape(tm // 8, 8, d), axis=0)
        i = pl.program_id(0)

        @pl.when(i == 0)
        def _():
            acc[...] = part

        @pl.when(i > 0)
        def _():
            acc[...] += part

        @pl.when(i == pl.num_programs(0) - 1)
        def _():
            loss_ref[...] = (0.5 / d) * jnp.sum(jnp.sum(acc[...], axis=1, keepdims=True), axis=0, keepdims=True)

    row = pl.BlockSpec((tm, d), lambda i: (i, 0))
    return pl.pallas_call(
        body, name=name, grid=(s // tm,),
        in_specs=[pl.BlockSpec((tm, k), lambda i: (i, 0)), pl.BlockSpec((k, d), lambda i: (0, 0)), row, row],
        out_specs=[row, row, pl.BlockSpec((1, 1), lambda i: (0, 0))],
        out_shape=[jax.ShapeDtypeStruct((s, d), F32), jax.ShapeDtypeStruct((s, d), BF16),
                   jax.ShapeDtypeStruct((1, 1), F32)],
        scratch_shapes=[pltpu.VMEM((8, d), F32)], compiler_params=_params(),
    )(act, w_ffn_out, x1, target)


LATE_GATHER = ("w_o_attn", "w_pw_conv", "w_out", "w_ffn_in", "w_ffn_out")
EARLY_REDUCE = LATE_GATHER


def _blocks_by_half(g):
    if g.ndim == 2:
        g = g.reshape(N_CHIPS, g.shape[0] // N_CHIPS, g.shape[1])
    return g.reshape(N_CHIPS, 2, g.shape[1] // 2, g.shape[2])


def _forward_backward(x, pos_col, target, wts, late_bufs, pos_arr):
    wts = dict(wts)
    consts = _rope_consts()
    bd = consts[3]
    qw2 = jnp.tile(wts["q_norm_w"], (1, LANES // HEAD_DIM))
    kw2 = jnp.tile(wts["k_norm_w"], (1, LANES // HEAD_DIM))
    qkv_w = 3 * N_SLOT_HEADS * HEAD_DIM
    conv_col0 = 3 * qkv_w
    ch = wts["conv_w"].shape[1]
    gate_col0 = conv_col0 + 2 * ch

    h = _rmsnorm_fwd(x, wts["norm1_w"], "rms1_fwd")
    gathering, started = _split_start(_gather_ici_rider(late_bufs, []), "late_gather_start", after=[wts["w_in"]])
    proj = _matmul(h, wts["w_in"], mode="nn", tm=512, tn=1920, tk=1024, out_dtype=F32, name="mm_proj",
                   b_blocked=True, after=[started])
    qn, kn = _qk_fwd(proj, pos_col, qw2, kw2, consts, "qk_fwd")
    attn, lse, attn_b = _attn_fwd(qn, kn, proj, "attn_fwd")
    late_bufs, _ = _split_wait(gathering, after=[attn_b], name="late_gather_wait")
    (u0, uc), late_bufs = _conv_fwd(proj, wts["conv_w"], wts["conv_b"], conv_col0, "conv_fwd",
                                    rider=_gather_forward_rider(late_bufs))
    for n, buf in zip(LATE_GATHER, late_bufs):
        full = buf.reshape(N_CHIPS, -1, buf.shape[3])
        wts[n] = full.reshape(-1, full.shape[2]) if n in ROW_SHARDED else full
    y_a = _matmul(attn_b, wts["w_o_attn"], mode="nn", tm=1024, tn=256, tk=512, out_dtype=F32, name="mm_ya",
                  b_blocked=True)
    u3 = _ln_silu_fwd(uc, wts["conv_ln_w"], wts["conv_ln_b"], "ln_fwd")
    y_b = _matmul(u3, wts["w_pw_conv"], mode="nn", tm=1024, tn=256, tk=512, out_dtype=F32, name="mm_yb",
                  b_blocked=True)
    mixed = _gate_fwd(proj, wts["b_gate"], y_a, y_b, gate_col0, "gate_fwd")
    x1, h2 = _out_proj_rmsnorm(mixed, wts["w_out"], x, wts["norm2_w"], "mm_x1_rms2")
    gate, up, act = _ffn_in_swiglu(h2, wts["w_ffn_in"], "mm_gu_swiglu")
    dy, dy_b16, loss = _ffn_out_loss(act, wts["w_ffn_out"], x1, target, "mm_x2_loss")

    g = {}
    by_chip = {}

    def pair_add(n, blocks, received):
        return _add_own_half(blocks, received, pos_arr, f"grads_pair_add_{n}")

    d_act = _matmul(dy_b16, wts["w_ffn_out"], mode="nt", tm=512, tn=1408, tk=1024, out_dtype=F32, name="mm_dact")
    g_ffn_out = _blocks_by_half(
        _matmul(act, dy_b16, mode="tn", tm=1408, tn=1024, tk=2048, out_dtype=F32, name="mm_dwffnout"))
    dgu, (received,) = _swiglu_bwd(gate, up, d_act, "swiglu_bwd",
                                   rider=_pair_exchange_rider([g_ffn_out], halved=True))
    to_send, own = pair_add("w_ffn_out", g_ffn_out, received)
    dh2, (by_chip["w_ffn_out"],) = _matmul(
        dgu, wts["w_ffn_in"], mode="nt", tm=1024, tn=1024, tk=1408, out_dtype=F32, name="mm_dh2", b_blocked=True,
        rider=_chip_exchange_rider([to_send], [own]))
    g_ffn_in = _blocks_by_half(_matmul(h2, dgu, mode="tn", tm=512, tn=1408, tk=2048, out_dtype=F32,
                                       name="mm_dwffnin", out_blocked=N_CHIPS))
    exchanging, started = _split_start(_pair_exchange_rider([g_ffn_in], halved=True), "grads_ffn_in_pair_start")
    dx1, dx1_b16, g["norm2_w"] = _rmsnorm_bwd(dh2, x1, wts["norm2_w"], dy, "rms2_bwd")
    d_mixed = _matmul(dx1_b16, wts["w_out"], mode="nt", tm=512, tn=1024, tk=1024, out_dtype=F32, name="mm_dmixed",
                      after=[started])
    g["w_out"] = _matmul(mixed, dx1_b16, mode="tn", tm=512, tn=1024, tk=2048, out_dtype=F32, name="mm_dwout")
    dgl, dy_a, dy_b, g["b_gate"] = _gate_bwd(d_mixed, proj, wts["b_gate"], y_a, y_b, gate_col0, "gate_bwd")
    (received,), (g_ffn_in,) = _split_wait(exchanging, after=[dgl], name="grads_ffn_in_pair_wait")
    ffn_in_to_send, ffn_in_own = pair_add("w_ffn_in", g_ffn_in, received)
    dattn = _matmul(dy_a, wts["w_o_attn"], mode="nt", tm=1024, tn=512, tk=256, out_dtype=F32, name="mm_dattn",
                    b_blocked=True)
    g["w_o_attn"] = _matmul(attn_b, dy_a, mode="tn", tm=512, tn=256, tk=2048, out_dtype=F32, name="mm_dwo",
                            out_blocked=N_CHIPS)
    du3 = _matmul(dy_b, wts["w_pw_conv"], mode="nt", tm=1024, tn=512, tk=256, out_dtype=F32, name="mm_du3",
                  b_blocked=True)
    g["w_pw_conv"] = _matmul(u3, dy_b, mode="tn", tm=512, tn=256, tk=2048, out_dtype=F32, name="mm_dwpw",
                             out_blocked=N_CHIPS)
    duc, g["conv_ln_w"], g["conv_ln_b"] = _ln_silu_bwd(du3, uc, wts["conv_ln_w"], wts["conv_ln_b"], "ln_bwd")

    small3 = ("w_out", "w_o_attn", "w_pw_conv")
    g_small3 = [_blocks_by_half(g.pop(n)) for n in small3]
    (da, db, g["conv_w"], g["conv_b"]), received = _conv_bwd(
        duc, u0, proj, wts["conv_w"], conv_col0, "conv_bwd", rider=_pair_exchange_rider(g_small3, halved=True))
    sums3 = [pair_add(n, gb, rv) for n, gb, rv in zip(small3, g_small3, received)]
    (dqn, dkn, dv), (by_chip["w_ffn_in"],) = _attn_bwd(
        qn, kn, proj, dattn, attn, lse, bd, "attn_bwd",
        rider=_chip_exchange_rider([ffn_in_to_send], [ffn_in_own]))
    (dproj, dqw, dkw), exchanged3 = _qk_bwd(
        dqn, dkn, dv, da, db, dgl, proj, pos_col, qw2, kw2, consts, "qk_bwd",
        rider=_chip_exchange_rider([s[0] for s in sums3], [s[1] for s in sums3]))
    by_chip.update(zip(small3, exchanged3))
    halves = [_sum_chips(by_chip[n], pos_arr, f"grads_chip_sum_{n}") for n in EARLY_REDUCE]
    g["q_norm_w"] = dqw[:, :HEAD_DIM]
    g["k_norm_w"] = dkw[:, :HEAD_DIM]

    c = pos_arr[0]
    rh = h.shape[1] // 2
    h_sibling = lax.dynamic_slice_in_dim(h, (1 - c) * rh, rh, axis=1)
    h_own = lax.dynamic_slice_in_dim(h, c * rh, rh, axis=1)
    g_sibling, shards = _matmul(h_sibling, dproj, mode="tn", tm=rh, tn=1920, tk=2048, out_dtype=F32,
                                name="mm_dwin_sibling", out_blocked=N_CHIPS, rider=_pair_gather_rider(halves))
    reduced = dict(zip(EARLY_REDUCE, shards))
    exchanging, started = _split_start(_pair_exchange_rider([g_sibling], halved=False), "grads_w_in_pair_start")
    g_own = _matmul(h_own, dproj, mode="tn", tm=rh, tn=1920, tk=2048, out_dtype=F32, name="mm_dwin_own",
                    out_blocked=N_CHIPS, after=[started])
    (from_sibling,), _ = _split_wait(exchanging, after=[g_own], name="grads_w_in_pair_wait")
    to_send, own = _add_own_half(g_own, from_sibling, pos_arr, "grads_pair_add_w_in")
    in_flight, started = _split_start(_chip_exchange_rider([to_send], [own]), "grads_w_in_exchange_start")
    dh = _matmul(dproj, wts["w_in"], mode="nt", tm=1024, tn=1024, tk=1920, out_dtype=F32, name="mm_dh",
                 b_blocked=True, after=[started])
    grad_x, _, g["norm1_w"] = _rmsnorm_bwd(dh, x, wts["norm1_w"], dx1, "rms1_bwd")
    return loss, grad_x, g, reduced, (in_flight, started)


def _mesh_pos():
    return lax.axis_index("x"), lax.axis_index("y"), lax.axis_index("c")


def _other_chips(x, y):
    return [(1 - x, y), (x, 1 - y), (1 - x, 1 - y)]


def _cast_into_slot(shard, chip_arr, dtype, name):
    r, c = shard.shape
    tr = r // 2 if r % 32 == 0 else r

    def body(chip_ref, s_ref, o_ref):
        del chip_ref
        o_ref[...] = s_ref[...].astype(dtype)

    return pl.pallas_call(
        body, name=name,
        grid_spec=pltpu.PrefetchScalarGridSpec(
            num_scalar_prefetch=1, grid=(r // tr,),
            in_specs=[pl.BlockSpec((tr, c), lambda i, chip_ref: (i, 0))],
            out_specs=pl.BlockSpec((None, tr, c), lambda i, chip_ref: (chip_ref[0], i, 0))),
        out_shape=jax.ShapeDtypeStruct((N_CHIPS, r, c), dtype), compiler_params=_params(),
    )(chip_arr, shard)


def _gather_both_legs_rider(big, small):
    nb = len(big)
    n = nb + len(small)

    def part(bufs, a, slot, half):
        return bufs[a].at[slot, half] if a < nb else bufs[a].at[slot]

    def start(r_in, bufs, sems):
        x, y, c = _mesh_pos()
        me = 2 * x + y
        chips = _other_chips(x, y)
        for a in range(n):
            for k in range(3):
                px, py = chips[k]
                pltpu.make_async_remote_copy(
                    src_ref=part(bufs, a, me, c), dst_ref=part(bufs, a, me, c), send_sem=sems[0].at[a, k],
                    recv_sem=sems[1].at[a, k], device_id=(px, py, c), device_id_type=MESH).start()

    def wait(r_in, bufs, sems):
        x, y, c = _mesh_pos()
        me = 2 * x + y
        chips = _other_chips(x, y)
        sibling = (x, y, 1 - c)
        forwards = []
        for a in range(n):
            for k in range(3):
                px, py = chips[k]
                slot = 2 * px + py
                pltpu.make_async_remote_copy(
                    src_ref=part(bufs, a, slot, c), dst_ref=part(bufs, a, slot, c), send_sem=sems[0].at[a, k],
                    recv_sem=sems[1].at[a, k], device_id=(px, py, c), device_id_type=MESH).wait_recv()
                if a < nb:
                    fwd = pltpu.make_async_remote_copy(
                        src_ref=part(bufs, a, slot, c), dst_ref=part(bufs, a, slot, c), send_sem=sems[2].at[a, k],
                        recv_sem=sems[3].at[a, k], device_id=sibling, device_id_type=MESH)
                    fwd.start()
                    forwards.append(fwd)
        for a in range(nb):
            for k in range(3):
                px, py = chips[k]
                slot = 2 * px + py
                pltpu.make_async_remote_copy(
                    src_ref=part(bufs, a, slot, 1 - c), dst_ref=part(bufs, a, slot, 1 - c),
                    send_sem=sems[2].at[a, k], recv_sem=sems[3].at[a, k], device_id=sibling,
                    device_id_type=MESH).wait_recv()
        for a in range(n):
            for k in range(3):
                px, py = chips[k]
                pltpu.make_async_remote_copy(
                    src_ref=part(bufs, a, me, c), dst_ref=part(bufs, a, me, c), send_sem=sems[0].at[a, k],
                    recv_sem=sems[1].at[a, k], device_id=(px, py, c), device_id_type=MESH).wait_send()
        for fwd in forwards:
            fwd.wait_send()

    ops = list(big) + list(small)
    return _Rider(ops, [jax.ShapeDtypeStruct(o.shape, o.dtype) for o in ops], {i: i for i in range(n)},
                  [pltpu.SemaphoreType.DMA((n, 3)), pltpu.SemaphoreType.DMA((n, 3)),
                   pltpu.SemaphoreType.DMA((nb, 3)), pltpu.SemaphoreType.DMA((nb, 3))], start, wait)


def _comm_call(rider, name):
    def body():
        pass

    return _pallas(body, name=name, grid=(1,), in_specs=[], out_specs=[], out_shape=[], operands=[],
                   rider=rider)[1]


def _gather_ici_rider(big, small):
    nb = len(big)
    n = nb + len(small)

    def copies(bufs, sems):
        x, y, c = _mesh_pos()
        me = 2 * x + y
        part = lambda a, slot: bufs[a].at[slot, c] if a < nb else bufs[a].at[slot]
        out = []
        for a in range(n):
            for k, (px, py) in enumerate(_other_chips(x, y)):
                send = functools.partial(
                    pltpu.make_async_remote_copy,
                    src_ref=part(a, me), dst_ref=part(a, me), send_sem=sems[0].at[a, k],
                    recv_sem=sems[1].at[a, k], device_id=(px, py, c), device_id_type=MESH)
                recv = functools.partial(
                    pltpu.make_async_remote_copy,
                    src_ref=part(a, 2 * px + py), dst_ref=part(a, 2 * px + py), send_sem=sems[0].at[a, k],
                    recv_sem=sems[1].at[a, k], device_id=(px, py, c), device_id_type=MESH)
                out.append((send, recv))
        return out

    def start(r_in, r_out, sems):
        for send, _ in copies(r_out, sems):
            send().start()

    def wait(r_in, r_out, sems):
        cps = copies(r_out, sems)
        for _, recv in cps:
            recv().wait_recv()
        for send, _ in cps:
            send().wait_send()

    ops = list(big) + list(small)
    return _Rider(ops, [jax.ShapeDtypeStruct(o.shape, o.dtype) for o in ops], {i: i for i in range(n)},
                  [pltpu.SemaphoreType.DMA((n, 3)), pltpu.SemaphoreType.DMA((n, 3))], start, wait)


def _gather_forward_rider(big):
    n = len(big)

    def copies(bufs, sems):
        x, y, c = _mesh_pos()
        out = []
        for a in range(n):
            for k, (px, py) in enumerate(_other_chips(x, y)):
                slot = 2 * px + py
                send = functools.partial(
                    pltpu.make_async_remote_copy,
                    src_ref=bufs[a].at[slot, c], dst_ref=bufs[a].at[slot, c], send_sem=sems[0].at[a, k],
                    recv_sem=sems[1].at[a, k], device_id=(x, y, 1 - c), device_id_type=MESH)
                recv = functools.partial(
                    pltpu.make_async_remote_copy,
                    src_ref=bufs[a].at[slot, 1 - c], dst_ref=bufs[a].at[slot, 1 - c], send_sem=sems[0].at[a, k],
                    recv_sem=sems[1].at[a, k], device_id=(x, y, 1 - c), device_id_type=MESH)
                out.append((send, recv))
        return out

    def start(r_in, r_out, sems):
        for send, _ in copies(r_out, sems):
            send().start()

    def wait(r_in, r_out, sems):
        cps = copies(r_out, sems)
        for _, recv in cps:
            recv().wait_recv()
        for send, _ in cps:
            send().wait_send()

    return _Rider(big, [jax.ShapeDtypeStruct(o.shape, o.dtype) for o in big], {i: i for i in range(n)},
                  [pltpu.SemaphoreType.DMA((n, 3)), pltpu.SemaphoreType.DMA((n, 3))], start, wait)


def _pair_exchange_rider(gs, halved):
    n = len(gs)

    def copies(r_in, r_out, sems):
        x, y, c = _mesh_pos()
        return [pltpu.make_async_remote_copy(
            src_ref=r_in[a].at[:, 1 - c] if halved else r_in[a], dst_ref=r_out[a], send_sem=sems[0].at[a],
            recv_sem=sems[1].at[a], device_id=(x, y, 1 - c), device_id_type=MESH) for a in range(n)]

    def start(r_in, r_out, sems):
        for cp in copies(r_in, r_out, sems):
            cp.start()

    def wait(r_in, r_out, sems):
        for cp in copies(r_in, r_out, sems):
            cp.wait()

    return _Rider(gs, [jax.ShapeDtypeStruct((g.shape[0],) + g.shape[-2:], g.dtype) for g in gs], {},
                  [pltpu.SemaphoreType.DMA((n,)), pltpu.SemaphoreType.DMA((n,))], start, wait)


def _chip_exchange_rider(to_send, by_chip, row_range=None):
    n = len(to_send)

    def copies(r_in, r_out, sems):
        x, y, c = _mesh_pos()
        me = 2 * x + y
        rows = (lambda ref: ref) if row_range is None else (lambda ref: ref.at[pl.ds(*row_range)])
        out = []
        for a in range(n):
            for k, (px, py) in enumerate(_other_chips(x, y)):
                send = functools.partial(
                    pltpu.make_async_remote_copy,
                    src_ref=rows(r_in[a].at[2 * px + py]), dst_ref=rows(r_out[a].at[me]),
                    send_sem=sems[0].at[a, k], recv_sem=sems[1].at[a, k], device_id=(px, py, c),
                    device_id_type=MESH)
                recv = functools.partial(
                    pltpu.make_async_remote_copy,
                    src_ref=rows(r_in[a].at[me]), dst_ref=rows(r_out[a].at[2 * px + py]),
                    send_sem=sems[0].at[a, k], recv_sem=sems[1].at[a, k], device_id=(px, py, c),
                    device_id_type=MESH)
                out.append((send, recv))
        return out

    def start(r_in, r_out, sems):
        for send, _ in copies(r_in, r_out, sems):
            send().start()

    def wait(r_in, r_out, sems):
        cps = copies(r_in, r_out, sems)
        for _, recv in cps:
            recv().wait_recv()
        for send, _ in cps:
            send().wait_send()

    return _Rider(list(to_send) + list(by_chip), [jax.ShapeDtypeStruct(b.shape, b.dtype) for b in by_chip],
                  {n + i: i for i in range(n)},
                  [pltpu.SemaphoreType.DMA((n, 3)), pltpu.SemaphoreType.DMA((n, 3))], start, wait)


HBM = pl.BlockSpec(memory_space=pltpu.HBM)
SEM = pl.BlockSpec(memory_space=pltpu.SEMAPHORE)


_IN_FLIGHT = pltpu.CompilerParams(has_side_effects=pltpu.SideEffectType.DATAFLOW_SIDE_EFFECTING)


class _FlatSems:
    def __init__(self, ref, shape):
        self.ref, self.shape = ref, shape

    @property
    def at(self):
        return self

    def __getitem__(self, idx):
        idx = idx if isinstance(idx, tuple) else (idx,)
        flat = 0
        for i, n in zip(idx, self.shape):
            flat = flat * n + i
        return self.ref.at[flat]


def _flat_sem_types(rider):
    return tuple(pltpu.SemaphoreType.DMA((int(np.prod(s.shape)),)) for s in rider.scratch)


def _as_rider_sems(rider, refs):
    return [_FlatSems(r, s.shape) for r, s in zip(refs, rider.scratch)]


def _split_start(rider, name, after=()):
    n_in, n_out, n_sem = len(rider.operands), len(rider.out_shapes), len(rider.scratch)
    n_after = len(after)
    fresh = [j for j in range(n_out) if j not in rider.aliases.values()]
    by_out = {j: i for i, j in rider.aliases.items()}

    def body(*refs):
        r_in = refs[:n_in]
        refs = refs[n_in + n_after:]
        sems = refs[:n_sem]
        thru = refs[n_sem:n_sem + n_in]
        fresh_refs = refs[n_sem + n_in:n_sem + n_in + len(fresh)]
        token = refs[-1]
        r_out = [thru[by_out[j]] if j in by_out else fresh_refs[fresh.index(j)] for j in range(n_out)]
        rider.start(r_in, r_out, _as_rider_sems(rider, sems))
        token[...] = jnp.zeros_like(token)

    res = pl.pallas_call(
        body, name=name,
        out_shape=_flat_sem_types(rider) + tuple(pltpu.HBM(o.shape, o.dtype) for o in rider.operands)
        + tuple(pltpu.HBM(rider.out_shapes[j].shape, rider.out_shapes[j].dtype) for j in fresh)
        + (jax.ShapeDtypeStruct((8, LANES), F32),),
        in_specs=(HBM,) * n_in + (ANY,) * n_after,
        out_specs=(SEM,) * n_sem + (HBM,) * (n_in + len(fresh)) + (pl.BlockSpec(memory_space=pltpu.VMEM),),
        input_output_aliases={i: n_sem + i for i in range(n_in)}, compiler_params=_IN_FLIGHT,
    )(*[pltpu.with_memory_space_constraint(o, pltpu.HBM) for o in rider.operands], *after)
    return (rider, res[:n_sem], res[n_sem:n_sem + n_in], res[n_sem + n_in:-1]), res[-1]


def _split_wait(handles, after, name):
    rider, sems, thru, fresh_arrays = handles
    n_in, n_out, n_sem = len(rider.operands), len(rider.out_shapes), len(rider.scratch)
    fresh = [j for j in range(n_out) if j not in rider.aliases.values()]
    by_out = {j: i for i, j in rider.aliases.items()}
    n_data = n_in + len(fresh)

    def body(*refs):
        r_in = refs[:n_in]
        fresh_refs = refs[n_in:n_data]
        sem_refs = refs[n_data:n_data + n_sem]
        r_out = [r_in[by_out[j]] if j in by_out else fresh_refs[fresh.index(j)] for j in range(n_out)]
        rider.wait(r_in, r_out, _as_rider_sems(rider, sem_refs))

    data = list(thru) + list(fresh_arrays)
    res = pl.pallas_call(
        body, name=name, out_shape=tuple(pltpu.HBM(d.shape, d.dtype) for d in data),
        in_specs=(HBM,) * n_data + (SEM,) * n_sem + (ANY,) * len(after), out_specs=(HBM,) * n_data,
        input_output_aliases={i: i for i in range(n_data)}, compiler_params=_IN_FLIGHT,
    )(*data, *sems, *after)
    return [res[by_out[j]] if j in by_out else res[n_in + fresh.index(j)] for j in range(n_out)], res[:n_in]


def _pair_gather_rider(bufs):
    n = len(bufs)

    def copies(r_out, sems):
        x, y, c = _mesh_pos()
        out = []
        for a in range(n):
            send = functools.partial(
                    pltpu.make_async_remote_copy,
                src_ref=r_out[a].at[c], dst_ref=r_out[a].at[c], send_sem=sems[0].at[a],
                recv_sem=sems[1].at[a], device_id=(x, y, 1 - c), device_id_type=MESH)
            recv = functools.partial(
                    pltpu.make_async_remote_copy,
                src_ref=r_out[a].at[1 - c], dst_ref=r_out[a].at[1 - c], send_sem=sems[0].at[a],
                recv_sem=sems[1].at[a], device_id=(x, y, 1 - c), device_id_type=MESH)
            out.append((send, recv))
        return out

    def start(r_in, r_out, sems):
        for send, _ in copies(r_out, sems):
            send().start()

    def wait(r_in, r_out, sems):
        cps = copies(r_out, sems)
        for _, recv in cps:
            recv().wait_recv()
        for send, _ in cps:
            send().wait_send()

    return _Rider(bufs, [jax.ShapeDtypeStruct(b.shape, b.dtype) for b in bufs], {i: i for i in range(n)},
                  [pltpu.SemaphoreType.DMA((n,)), pltpu.SemaphoreType.DMA((n,))], start, wait)


def _add_own_half(g, recv, pos_arr, name):
    nb, rh, cols = g.shape[0], g.shape[-2], g.shape[-1]

    def body(pos_ref, g_ref, r_ref, send_ref, own_ref):
        s = (g_ref[...] + r_ref[...]).astype(BF16)
        send_ref[...] = s

        @pl.when(pl.program_id(0) == pos_ref[1])
        def _():
            own_ref[...] = s

    blk = pl.BlockSpec((None, rh, cols), lambda j, pos_ref: (j, 0, 0))
    g_spec = blk if g.ndim == 3 else pl.BlockSpec((None, None, rh, cols),
                                                   lambda j, pos_ref: (j, pos_ref[0], 0, 0))
    shape = jax.ShapeDtypeStruct((nb, rh, cols), BF16)
    return pl.pallas_call(
        body, name=name,
        grid_spec=pltpu.PrefetchScalarGridSpec(
            num_scalar_prefetch=1, grid=(nb,), in_specs=[g_spec, blk],
            out_specs=[blk, pl.BlockSpec((None, rh, cols), lambda j, pos_ref: (pos_ref[1], 0, 0))]),
        out_shape=[shape, shape], compiler_params=_params(),
    )(pos_arr, g, recv)


def _sum_chips(gath, pos_arr, name):
    nb, rh, cols = gath.shape

    def body(pos_ref, a_ref, b_ref, c_ref, d_ref, o_ref):
        del pos_ref
        o_ref[...] = ((a_ref[...].astype(F32) + b_ref[...].astype(F32)) + c_ref[...].astype(F32)) \
            + d_ref[...].astype(F32)

    tr = rh // 2 if (rh // 2) % 16 == 0 else rh
    specs = [pl.BlockSpec((None, tr, cols), functools.partial(lambda i, pos_ref, j: (j, i, 0), j=j))
             for j in range(nb)]
    return pl.pallas_call(
        body, name=name,
        grid_spec=pltpu.PrefetchScalarGridSpec(
            num_scalar_prefetch=1, grid=(rh // tr,), in_specs=specs,
            out_specs=pl.BlockSpec((None, tr, cols), lambda i, pos_ref: (pos_ref[0], i, 0))),
        out_shape=jax.ShapeDtypeStruct((2, rh, cols), F32), compiler_params=_params(),
    )(pos_arr, gath, gath, gath, gath)


def _small_allreduce(v, name, rider=None, after=()):
    n = v.shape[0]
    n_dev = 8

    def body(v_ref, o_ref, buf, send_sems, recv_sems):
        x, y, c = _mesh_pos()
        me = 4 * x + 2 * y + c
        buf[me] = v_ref[...]
        sends = []
        peers = []
        for r in range(1, n_dev):
            px = 1 - x if r & 4 else x
            py = 1 - y if r & 2 else y
            pc = 1 - c if r & 1 else c
            peers.append((px, py, pc))
            cp = pltpu.make_async_remote_copy(
                src_ref=v_ref, dst_ref=buf.at[me], send_sem=send_sems.at[r - 1],
                recv_sem=recv_sems.at[r - 1], device_id=(px, py, pc), device_id_type=MESH)
            cp.start()
            sends.append(cp)
        for r, (px, py, pc) in enumerate(peers):
            pltpu.make_async_remote_copy(
                src_ref=v_ref, dst_ref=buf.at[4 * px + 2 * py + pc], send_sem=send_sems.at[r],
                recv_sem=recv_sems.at[r], device_id=(px, py, pc), device_id_type=MESH).wait_recv()
        for cp in sends:
            cp.wait_send()
        acc = buf[0]
        for i in range(1, n_dev):
            acc = acc + buf[i]
        o_ref[...] = acc

    whole = pl.BlockSpec(v.shape, lambda i: (0, 0))
    return _pallas(
        body, name=name, grid=(1,), in_specs=[whole], out_specs=whole,
        out_shape=jax.ShapeDtypeStruct(v.shape, v.dtype), operands=[v],
        scratch_shapes=[pltpu.VMEM((n_dev, n, LANES), F32), pltpu.SemaphoreType.DMA((n_dev - 1,)),
                        pltpu.SemaphoreType.DMA((n_dev - 1,))],
        rider=rider, after=after)


def _adamw_math(w, g, m, v):
    m = ADAM_B1 * m + (1.0 - ADAM_B1) * g
    v = ADAM_B2 * v + (1.0 - ADAM_B2) * (g * g)
    m_hat = m / (1.0 - ADAM_B1 ** ADAM_STEP)
    v_hat = v / (1.0 - ADAM_B2 ** ADAM_STEP)
    delta = -ADAM_LR * (m_hat / (jnp.sqrt(v_hat) + ADAM_EPS) + ADAM_WD * w)
    return delta, m, v


def _adamw(w, g, m, v, name, after=()):
    r, c = w.shape
    tr = 128 if r % 128 == 0 else 64
    assert r % tr == 0

    def body(w_ref, g_ref, m_ref, v_ref, go_ref, d_ref, mo_ref, vo_ref):
        gv = g_ref[...]
        d, mn, vn = _adamw_math(w_ref[...], gv, m_ref[...], v_ref[...])
        go_ref[...] = gv
        d_ref[...] = d
        mo_ref[...] = mn
        vo_ref[...] = vn

    blk = pl.BlockSpec((tr, c), lambda i: (i, 0))
    return _pallas(body, name=name, grid=(r // tr,), in_specs=[blk] * 4, out_specs=[blk] * 4,
                   out_shape=[jax.ShapeDtypeStruct((r, c), F32)] * 4, operands=[w, g, m, v], after=after)


def _adamw_small(ws, gs, ms, vs, name):
    n = len(ws)

    def body(*refs):
        w_r, g_r, m_r, v_r = refs[:n], refs[n:2 * n], refs[2 * n:3 * n], refs[3 * n:4 * n]
        d_o, m_o, v_o = refs[4 * n:5 * n], refs[5 * n:6 * n], refs[6 * n:7 * n]
        for i in range(n):
            d, mn, vn = _adamw_math(w_r[i][...], g_r[i][...], m_r[i][...], v_r[i][...])
            d_o[i][...] = d
            m_o[i][...] = mn
            v_o[i][...] = vn

    specs = [pl.BlockSpec(w.shape, lambda i: (0, 0)) for w in ws]
    shapes = [jax.ShapeDtypeStruct(w.shape, F32) for w in ws]
    outs = pl.pallas_call(
        body, name=name, grid=(1,), in_specs=specs * 4, out_specs=specs * 3, out_shape=shapes * 3,
        compiler_params=_params(),
    )(*ws, *gs, *ms, *vs)
    return outs[:n], outs[n:2 * n], outs[2 * n:]


BIG = ("w_in", "w_o_attn", "w_pw_conv", "w_out", "w_ffn_in", "w_ffn_out")
ROW_SHARDED = ("w_out", "w_ffn_out")
SMALL = ("norm1_w", "b_gate", "q_norm_w", "k_norm_w", "conv_w", "conv_b", "conv_ln_w", "conv_ln_b", "norm2_w")
ORDER = ("norm1_w", "w_in", "b_gate", "q_norm_w", "k_norm_w", "w_o_attn", "conv_w", "conv_b", "conv_ln_w",
         "conv_ln_b", "w_pw_conv", "w_out", "norm2_w", "w_ffn_in", "w_ffn_out")
PACK_TILE = 8 * LANES


def _pack_small(parts):
    rows = []
    for p in parts:
        flat = p.reshape(-1)
        pad = (-flat.shape[0]) % PACK_TILE
        rows.append(jnp.pad(flat, (0, pad)).reshape(-1, LANES))
    return jnp.concatenate(rows, axis=0)


def _unpack_small(packed, shapes):
    out, row = [], 0
    for shp in shapes:
        size = int(np.prod(shp))
        nrow = -(-size // PACK_TILE) * (PACK_TILE // LANES)
        out.append(packed[row:row + nrow].reshape(-1)[:size].reshape(shp))
        row += nrow
    return out


def kernel(x, positions, norm1_w, w_in, b_gate, q_norm_w, k_norm_w, w_o_attn, conv_w, conv_b, conv_ln_w, conv_ln_b, w_pw_conv, w_out, norm2_w, w_ffn_in, w_ffn_out, loss_target, m_norm1_w, m_w_in, m_b_gate, m_q_norm_w, m_k_norm_w, m_w_o_attn, m_conv_w, m_conv_b, m_conv_ln_w, m_conv_ln_b, m_w_pw_conv, m_w_out, m_norm2_w, m_w_ffn_in, m_w_ffn_out, v_norm1_w, v_w_in, v_b_gate, v_q_norm_w, v_k_norm_w, v_w_o_attn, v_conv_w, v_conv_b, v_conv_ln_w, v_conv_ln_b, v_w_pw_conv, v_w_out, v_norm2_w, v_w_ffn_in, v_w_ffn_out):
    w = dict(norm1_w=norm1_w, w_in=w_in, b_gate=b_gate, q_norm_w=q_norm_w, k_norm_w=k_norm_w, w_o_attn=w_o_attn,
             conv_w=conv_w, conv_b=conv_b, conv_ln_w=conv_ln_w, conv_ln_b=conv_ln_b, w_pw_conv=w_pw_conv,
             w_out=w_out, norm2_w=norm2_w, w_ffn_in=w_ffn_in, w_ffn_out=w_ffn_out)
    m = dict(norm1_w=m_norm1_w, w_in=m_w_in, b_gate=m_b_gate, q_norm_w=m_q_norm_w, k_norm_w=m_k_norm_w,
             w_o_attn=m_w_o_attn, conv_w=m_conv_w, conv_b=m_conv_b, conv_ln_w=m_conv_ln_w,
             conv_ln_b=m_conv_ln_b, w_pw_conv=m_w_pw_conv, w_out=m_w_out, norm2_w=m_norm2_w,
             w_ffn_in=m_w_ffn_in, w_ffn_out=m_w_ffn_out)
    v = dict(norm1_w=v_norm1_w, w_in=v_w_in, b_gate=v_b_gate, q_norm_w=v_q_norm_w, k_norm_w=v_k_norm_w,
             w_o_attn=v_w_o_attn, conv_w=v_conv_w, conv_b=v_conv_b, conv_ln_w=v_conv_ln_w,
             conv_ln_b=v_conv_ln_b, w_pw_conv=v_w_pw_conv, w_out=v_w_out, norm2_w=v_norm2_w,
             w_ffn_in=v_w_ffn_in, w_ffn_out=v_w_ffn_out)
    cx, cy, cc = _mesh_pos()
    chip = 2 * cx + cy

    chip_arr = chip.reshape(1).astype(jnp.int32)
    pos_arr = jnp.stack([cc, chip]).astype(jnp.int32)
    bufs = {}
    for n in BIG:
        buf = _cast_into_slot(w[n][0], chip_arr, BF16, f"cast_{n}")
        bufs[n] = buf.reshape(N_CHIPS, 2, buf.shape[1] // 2, buf.shape[2])
    small_bufs = [_cast_into_slot(w[n][0], chip_arr, F32, f"slot_{n}") for n in ("conv_w", "b_gate")]
    w_in_buf, conv_w_buf, b_gate_buf = _comm_call(_gather_both_legs_rider([bufs["w_in"]], small_bufs),
                                                  "allgather_w_in")
    wts = dict(w_in=w_in_buf.reshape(N_CHIPS, -1, w_in_buf.shape[3]),
               conv_w=conv_w_buf.transpose(1, 0, 2).reshape(CONV_WIDTH, -1),
               b_gate=b_gate_buf.transpose(1, 0, 2).reshape(2, 1, -1),
               norm1_w=norm1_w, q_norm_w=q_norm_w, k_norm_w=k_norm_w, conv_b=conv_b, conv_ln_w=conv_ln_w,
               conv_ln_b=conv_ln_b, norm2_w=norm2_w)

    loss, grad_x, g, reduced, w_in_in_flight = _forward_backward(
        x[0], positions.reshape(-1, 1), loss_target[0], wts, [bufs[n] for n in LATE_GATHER], pos_arr)
    grads = {n: b.reshape(-1, b.shape[2]) for n, b in reduced.items()}

    w_in_in_flight, started = w_in_in_flight
    delta, new_m, new_v = {}, {}, {}
    for n in EARLY_REDUCE:
        grads[n], delta[n], new_m[n], new_v[n] = _adamw(w[n][0], grads[n], m[n][0], v[n][0], f"adamw_{n}",
                                                        after=[started])
    small_parts = [loss] + [g[n] for n in SMALL]
    small_shapes = [p.shape for p in small_parts]
    summed = _small_allreduce(_pack_small(small_parts), "small_allreduce",
                              after=[delta[n] for n in EARLY_REDUCE])
    reduced = _unpack_small(summed, small_shapes)
    loss_total = reduced[0].reshape(())
    for n, r in zip(SMALL, reduced[1:]):
        grads[n] = r
    ch_shard = conv_w.shape[2]
    grads["conv_w"] = lax.dynamic_slice_in_dim(grads["conv_w"], chip * ch_shard, ch_shard, axis=1)
    d_shard = b_gate.shape[2]
    grads["b_gate"] = lax.dynamic_slice_in_dim(grads["b_gate"], chip * d_shard, d_shard, axis=1)

    (by_chip_w_in,), _ = _split_wait(w_in_in_flight, after=[delta[n] for n in EARLY_REDUCE] + [summed],
                                     name="grads_w_in_exchange_wait")
    half_w_in = _sum_chips(by_chip_w_in, pos_arr, "grads_chip_sum_w_in")
    (shard_w_in,) = _comm_call(_pair_gather_rider([half_w_in]), "grads_pair_gather_w_in")
    grads["w_in"], delta["w_in"], new_m["w_in"], new_v["w_in"] = _adamw(
        w["w_in"][0], shard_w_in.reshape(-1, shard_w_in.shape[2]), m["w_in"][0], v["w_in"][0], "adamw_w_in")
    flat2 = lambda a: a.reshape(-1, a.shape[-1])
    d_s, m_s, v_s = _adamw_small([flat2(w[n]) for n in SMALL], [flat2(grads[n]) for n in SMALL],
                                 [flat2(m[n]) for n in SMALL], [flat2(v[n]) for n in SMALL], "adamw_small")
    for i, n in enumerate(SMALL):
        delta[n], new_m[n], new_v[n] = d_s[i], m_s[i], v_s[i]

    shaped = lambda d, n: d[n].reshape(w[n].shape)
    return (loss_total, grad_x[None], *[shaped(grads, n) for n in ORDER], *[shaped(delta, n) for n in ORDER],
            *[shaped(new_m, n) for n in ORDER], *[shaped(new_v, n) for n in ORDER])
```

```python
import functools

import numpy as np
import jax
import jax.numpy as jnp
from jax import lax
from jax.experimental import pallas as pl
from jax.experimental.pallas import tpu as pltpu

F32 = jnp.float32
BF16 = jnp.bfloat16
MESH = pl.DeviceIdType.MESH
ANY = pl.BlockSpec(memory_space=pl.ANY)

HEAD_DIM = 64
N_SLOT_HEADS = 8
DILATIONS = (1, 4, 16)
HALF_SPAN = 64
ROPE_THETA = 500000.0
ROT_DIM = 16
CONV_WIDTH = 31
EPS = 1e-6
NEG_INF = -1e30
ADAM_LR, ADAM_B1, ADAM_B2, ADAM_EPS, ADAM_WD, ADAM_STEP = 0.001, 0.9, 0.999, 1e-08, 0.01, 10

LANES = 128
QBLK = 128
KWIN = QBLK + 2 * HALF_SPAN
VMEM_LIMIT = 48 * 1024 * 1024
N_CHIPS = 4


def _params(**kw):
    return pltpu.CompilerParams(vmem_limit_bytes=VMEM_LIMIT, **kw)


class _Rider:
    def __init__(self, operands, out_shapes, aliases, scratch, start, wait):
        self.operands, self.out_shapes, self.aliases = list(operands), list(out_shapes), dict(aliases)
        self.scratch, self.start, self.wait = list(scratch), start, wait


def _pallas(body, *, name, grid, in_specs, out_specs, out_shape, operands, scratch_shapes=(), aliases=None,
            rider=None, after=()):
    single = not isinstance(out_specs, (list, tuple))
    out_specs_l = [out_specs] if single else list(out_specs)
    out_shape_l = [out_shape] if single else list(out_shape)
    aliases = dict(aliases or {})

    def call(fn, all_in_specs, all_out_specs, all_out_shape, all_scratch, all_aliases, all_operands):
        return pl.pallas_call(
            fn, name=name, grid=grid, in_specs=all_in_specs, out_specs=all_out_specs, out_shape=all_out_shape,
            scratch_shapes=all_scratch, input_output_aliases=all_aliases, compiler_params=_params(),
        )(*all_operands)

    if rider is None:
        n_main = len(in_specs)

        def ordered(*refs):
            body(*refs[:n_main], *refs[n_main + len(after):])

        res = call(ordered if after else body, list(in_specs) + [ANY] * len(after), out_specs_l, out_shape_l,
                   list(scratch_shapes), aliases, list(operands) + list(after))
        return res[0] if single else res
    assert not after
    n_in, n_rin = len(in_specs), len(rider.operands)
    n_out, n_rout = len(out_specs_l), len(rider.out_shapes)
    n_sc = len(scratch_shapes)

    def wrapped(*refs):
        main_in, r_in = refs[:n_in], refs[n_in:n_in + n_rin]
        o0 = n_in + n_rin
        main_out, r_out = refs[o0:o0 + n_out], refs[o0 + n_out:o0 + n_out + n_rout]
        s0 = o0 + n_out + n_rout
        main_sc, r_sc = refs[s0:s0 + n_sc], refs[s0 + n_sc:]
        ids = [pl.program_id(d) for d in range(len(grid))]
        first = functools.reduce(jnp.logical_and, [i == 0 for i in ids])
        last = functools.reduce(jnp.logical_and, [i == n - 1 for i, n in zip(ids, grid)])

        @pl.when(first)
        def _():
            rider.start(r_in, r_out, r_sc)

        body(*main_in, *main_out, *main_sc)

        @pl.when(last)
        def _():
            rider.wait(r_in, r_out, r_sc)

    for src, dst in rider.aliases.items():
        aliases[n_in + src] = n_out + dst
    res = call(wrapped, list(in_specs) + [ANY] * n_rin, out_specs_l + [ANY] * n_rout,
               out_shape_l + rider.out_shapes, list(scratch_shapes) + rider.scratch, aliases,
               list(operands) + rider.operands)
    main = res[:n_out]
    return (main[0] if single else main), res[n_out:]


def _matmul(a, b, *, mode, tm, tn, tk, out_dtype, name, b_blocked=False,
            out_blocked=None, cols_outer=False, rider=None, after=()):
    a_shape = a.shape
    if mode == "nn":
        m_dim, k_dim = a_shape
        n_dim = b.shape[0] * b.shape[2] if b_blocked else b.shape[1]
        rows, cols, red = m_dim, n_dim, k_dim
    elif mode == "nt":
        m_dim, n_dim = a_shape
        k_dim = b.shape[1] if b_blocked else b.shape[0]
        rows, cols, red = m_dim, k_dim, n_dim
    else:
        m_dim, k_dim = a_shape
        n_dim = b.shape[1]
        rows, cols, red = k_dim, n_dim, m_dim
    assert rows % tm == 0 and cols % tn == 0 and red % tk == 0, (name, rows, cols, red)
    ni, nj, nk = rows // tm, cols // tn, red // tk

    if mode == "nn":
        a_spec = pl.BlockSpec((tm, tk), lambda i, j, k: (i, k))
        if b_blocked:
            per = b.shape[2] // tn
            b_spec = pl.BlockSpec((None, tk, tn), lambda i, j, k: (j // per, k, j % per))
        else:
            b_spec = pl.BlockSpec((tk, tn), lambda i, j, k: (k, j))
        dims = (((1,), (0,)), ((), ()))
    elif mode == "nt":
        a_spec = pl.BlockSpec((tm, tk), lambda i, j, k: (i, k))
        if b_blocked:
            per = b.shape[2] // tk
            b_spec = pl.BlockSpec((None, tn, tk), lambda i, j, k: (k // per, j, k % per))
        else:
            b_spec = pl.BlockSpec((tn, tk), lambda i, j, k: (j, k))
        dims = (((1,), (1,)), ((), ()))
    else:
        a_spec = pl.BlockSpec((tk, tm), lambda i, j, k: (k, i))
        b_spec = pl.BlockSpec((tk, tn), lambda i, j, k: (k, j))
        dims = (((0,), (0,)), ((), ()))

    if out_blocked:
        per_o = (cols // out_blocked) // tn
        out_spec = pl.BlockSpec((None, tm, tn), lambda i, j, k: (j // per_o, i, j % per_o))
        out_shape = jax.ShapeDtypeStruct((out_blocked, rows, cols // out_blocked), out_dtype)
    else:
        out_spec = pl.BlockSpec((tm, tn), lambda i, j, k: (i, j))
        out_shape = jax.ShapeDtypeStruct((rows, cols), out_dtype)

    def body(a_ref, b_ref, o_ref, *acc):
        prod = lax.dot_general(a_ref[...], b_ref[...], dims, preferred_element_type=F32)
        if nk == 1:
            o_ref[...] = prod.astype(out_dtype)
        else:
            acc_ref, = acc
            k = pl.program_id(2)

            @pl.when(k == 0)
            def _():
                acc_ref[...] = prod

            @pl.when(k > 0)
            def _():
                acc_ref[...] += prod

            @pl.when(k == nk - 1)
            def _():
                o_ref[...] = acc_ref[...].astype(out_dtype)

    scratch = [pltpu.VMEM((tm, tn), F32)] if nk > 1 else []
    grid = (ni, nj, nk)
    if cols_outer:
        swap = lambda spec: pl.BlockSpec(spec.block_shape, lambda j, i, k, f=spec.index_map: f(i, j, k))
        a_spec, b_spec, out_spec, grid = swap(a_spec), swap(b_spec), swap(out_spec), (nj, ni, nk)
    return _pallas(body, name=name, grid=grid, in_specs=[a_spec, b_spec], out_specs=out_spec,
                   out_shape=out_shape, operands=[a, b], scratch_shapes=scratch, rider=rider, after=after)


def _rmsnorm_fwd(x, w, name):
    s, d = x.shape
    tm = 256

    def body(x_ref, w_ref, o_ref):
        xv = x_ref[...]
        rstd = lax.rsqrt(jnp.mean(xv * xv, axis=-1, keepdims=True) + EPS)
        o_ref[...] = (xv * rstd * w_ref[...]).astype(BF16)

    return pl.pallas_call(
        body, name=name, grid=(s // tm,),
        in_specs=[pl.BlockSpec((tm, d), lambda i: (i, 0)), pl.BlockSpec((1, d), lambda i: (0, 0))],
        out_specs=pl.BlockSpec((tm, d), lambda i: (i, 0)),
        out_shape=jax.ShapeDtypeStruct((s, d), BF16), compiler_params=_params(),
    )(x, w)


def _rmsnorm_bwd(dh, x, w, dres, name, rider=None):
    s, d = x.shape
    tm = 256

    def body(dh_ref, x_ref, w_ref, dres_ref, dx_ref, dxb_ref, dw_ref):
        xv = x_ref[...]
        rstd = lax.rsqrt(jnp.mean(xv * xv, axis=-1, keepdims=True) + EPS)
        xhat = xv * rstd
        dhv = dh_ref[...]
        g = dhv * w_ref[...]
        dx = rstd * (g - xhat * jnp.mean(g * xhat, axis=-1, keepdims=True)) + dres_ref[...]
        dx_ref[...] = dx
        dxb_ref[...] = dx.astype(BF16)
        part = jnp.sum(dhv * xhat, axis=0, keepdims=True)

        @pl.when(pl.program_id(0) == 0)
        def _():
            dw_ref[...] = part

        @pl.when(pl.program_id(0) > 0)
        def _():
            dw_ref[...] += part

    row = pl.BlockSpec((tm, d), lambda i: (i, 0))
    vec = pl.BlockSpec((1, d), lambda i: (0, 0))
    return _pallas(
        body, name=name, grid=(s // tm,), in_specs=[row, row, vec, row], out_specs=[row, row, vec],
        out_shape=[jax.ShapeDtypeStruct((s, d), F32), jax.ShapeDtypeStruct((s, d), BF16),
                   jax.ShapeDtypeStruct((1, d), F32)],
        operands=[dh, x, w, dres], rider=rider)


def _rope_consts():
    lane = np.arange(LANES)
    in_head = lane % HEAD_DIM
    inv_freq = ROPE_THETA ** (-jnp.arange(0, ROT_DIM, 2, dtype=F32) / ROT_DIM)
    invf = jnp.where(jnp.asarray(in_head < ROT_DIM), jnp.tile(inv_freq, LANES // (ROT_DIM // 2)), 0.0)
    m_a = np.where(in_head < ROT_DIM // 2, -1.0, 0.0).astype(np.float32)
    m_b = np.where((in_head >= ROT_DIM // 2) & (in_head < ROT_DIM), 1.0, 0.0).astype(np.float32)
    block_diag = (lane[:, None] // HEAD_DIM == lane[None, :] // HEAD_DIM).astype(np.float32)
    return (invf.reshape(1, LANES).astype(F32), jnp.asarray(m_a).reshape(1, LANES),
            jnp.asarray(m_b).reshape(1, LANES), jnp.asarray(block_diag, dtype=BF16))


def _head_sums(v, bd):
    hi = v.astype(BF16)
    lo = (v - hi.astype(F32)).astype(BF16)
    return jnp.dot(hi, bd, preferred_element_type=F32) + jnp.dot(lo, bd, preferred_element_type=F32)


def _qk_fwd(proj, pos_col, qw2, kw2, consts, name, rider=None):
    s = proj.shape[0]
    width = 3 * N_SLOT_HEADS * HEAD_DIM
    tm = 128
    invf, m_a, m_b, bd = consts
    scale = HEAD_DIM ** -0.5

    def body(q_ref, k_ref, pos_ref, qw_ref, kw_ref, invf_ref, ma_ref, mb_ref, bd_ref, qo_ref, ko_ref):
        ang = pos_ref[...].astype(F32) * invf_ref[...]
        cos = jnp.cos(ang)
        sin = jnp.sin(ang)
        s_a = sin * ma_ref[...]
        s_b = sin * mb_ref[...]
        bdv = bd_ref[...]
        for src, w_ref, dst, sc in ((q_ref, qw_ref, qo_ref, scale), (k_ref, kw_ref, ko_ref, 1.0)):
            for cb in range(width // LANES):
                cols = slice(cb * LANES, (cb + 1) * LANES)
                t = src[:, cols]
                rstd = lax.rsqrt(_head_sums(t * t, bdv) * (1.0 / HEAD_DIM) + EPS)
                y = t * rstd * w_ref[...]
                r = y * cos + pltpu.roll(y, LANES - 8, axis=1) * s_a + pltpu.roll(y, 8, axis=1) * s_b
                dst[:, cols] = r * sc if sc != 1.0 else r

    vec = pl.BlockSpec((1, LANES), lambda i: (0, 0))
    return _pallas(
        body, name=name, grid=(s // tm,),
        in_specs=[pl.BlockSpec((tm, width), lambda i: (i, 0)), pl.BlockSpec((tm, width), lambda i: (i, 1)),
                  pl.BlockSpec((tm, 1), lambda i: (i, 0)), vec, vec, vec, vec, vec,
                  pl.BlockSpec((LANES, LANES), lambda i: (0, 0))],
        out_specs=[pl.BlockSpec((tm, width), lambda i: (i, 0))] * 2,
        out_shape=[jax.ShapeDtypeStruct((s, width), F32)] * 2,
        operands=[proj, proj, pos_col, qw2, kw2, invf, m_a, m_b, bd], rider=rider)


def _qk_bwd(dqn, dkn, dv, da, db, dgl, proj, pos_col, qw2, kw2, consts, name, rider=None):
    s = proj.shape[0]
    width = 3 * N_SLOT_HEADS * HEAD_DIM
    ch = da.shape[1]
    gate_w = dgl.shape[1]
    out_w = 3 * width + 2 * ch + gate_w
    assert out_w == proj.shape[1]
    tm = 128
    invf, m_a, m_b, bd = consts
    scale = HEAD_DIM ** -0.5

    def body(dq_ref, dk_ref, dv_ref, da_ref, db_ref, dgl_ref, q_ref, k_ref, pos_ref, qw_ref, kw_ref,
             invf_ref, ma_ref, mb_ref, bd_ref, out_ref, dqw_ref, dkw_ref):
        ang = pos_ref[...].astype(F32) * invf_ref[...]
        cos = jnp.cos(ang)
        sin = jnp.sin(ang)
        s_a = sin * ma_ref[...]
        s_b = sin * mb_ref[...]
        bdv = bd_ref[...]
        first = pl.program_id(0) == 0
        for src, dsrc, w_ref, col0, dw_ref, sc in ((q_ref, dq_ref, qw_ref, 0, dqw_ref, scale),
                                                   (k_ref, dk_ref, kw_ref, width, dkw_ref, 1.0)):
            dw_acc = jnp.zeros((1, LANES), F32)
            for cb in range(width // LANES):
                cols = slice(cb * LANES, (cb + 1) * LANES)
                t = src[:, cols]
                dr = dsrc[:, cols]
                if sc != 1.0:
                    dr = dr * sc
                dy = dr * cos + pltpu.roll(dr * s_a, 8, axis=1) + pltpu.roll(dr * s_b, LANES - 8, axis=1)
                rstd = lax.rsqrt(_head_sums(t * t, bdv) * (1.0 / HEAD_DIM) + EPS)
                xhat = t * rstd
                g = dy * w_ref[...]
                dt = rstd * (g - xhat * (_head_sums(g * xhat, bdv) * (1.0 / HEAD_DIM)))
                out_ref[:, col0 + cb * LANES: col0 + (cb + 1) * LANES] = dt.astype(BF16)
                dw_acc = dw_acc + jnp.sum(dy * xhat, axis=0, keepdims=True)
            dw_acc = dw_acc + pltpu.roll(dw_acc, HEAD_DIM, axis=1)

            @pl.when(first)
            def _(dw_ref=dw_ref, dw_acc=dw_acc):
                dw_ref[...] = dw_acc

            @pl.when(jnp.logical_not(first))
            def _(dw_ref=dw_ref, dw_acc=dw_acc):
                dw_ref[...] += dw_acc
        out_ref[:, 2 * width: 3 * width] = dv_ref[...].astype(BF16)
        out_ref[:, 3 * width: 3 * width + ch] = da_ref[...]
        out_ref[:, 3 * width + ch: 3 * width + 2 * ch] = db_ref[...]
        out_ref[:, 3 * width + 2 * ch: out_w] = dgl_ref[...]

    vec = pl.BlockSpec((1, LANES), lambda i: (0, 0))
    blk = lambda c: pl.BlockSpec((tm, width), lambda i: (i, c))
    cblk = pl.BlockSpec((tm, ch), lambda i: (i, 0))
    return _pallas(
        body, name=name, grid=(s // tm,),
        in_specs=[blk(0), blk(0), blk(0), cblk, cblk, pl.BlockSpec((tm, gate_w), lambda i: (i, 0)),
                  blk(0), blk(1), pl.BlockSpec((tm, 1), lambda i: (i, 0)), vec, vec, vec, vec, vec,
                  pl.BlockSpec((LANES, LANES), lambda i: (0, 0))],
        out_specs=[pl.BlockSpec((tm, out_w), lambda i: (i, 0)), vec, vec],
        out_shape=[jax.ShapeDtypeStruct((s, out_w), BF16)] + [jax.ShapeDtypeStruct((1, LANES), F32)] * 2,
        operands=[dqn, dkn, dv, da, db, dgl, proj, proj, pos_col, qw2, kw2, invf, m_a, m_b, bd],
        rider=rider)


def _row_chunks(n_rows, fn, chunk=256):
    def step(i, c):
        fn(pl.ds(pl.multiple_of(i * chunk, chunk), chunk))
        return c
    lax.fori_loop(0, n_rows // chunk, step, 0)


def _to_residue_major(dst, src, s, d, dst_off=0, cast=None):
    seq = s // d
    for r in range(d):
        v = src[...] if d == 1 else src[pl.ds(r, seq, stride=d), :]
        dst[dst_off + r * seq: dst_off + (r + 1) * seq, :] = v if cast is None else v.astype(cast)


def _from_residue_major(dst, src, s, d, src_off=0):
    seq = s // d
    for r in range(d):
        v = src[src_off + r * seq: src_off + (r + 1) * seq, :]
        if d == 1:
            dst[...] = v
        else:
            dst[pl.ds(r, seq, stride=d), :] = v


def _band_bias():
    qi = lax.broadcasted_iota(jnp.int32, (QBLK, KWIN), 0)
    kj = lax.broadcasted_iota(jnp.int32, (QBLK, KWIN), 1)
    return jnp.where(jnp.abs(kj - HALF_SPAN - qi) <= HALF_SPAN, 0.0, NEG_INF).astype(F32)


def _range_bias(base, seq):
    kj = lax.broadcasted_iota(jnp.int32, (1, KWIN), 1)
    lo = (base & -seq) - base + HALF_SPAN
    return jnp.where((kj >= lo) & (kj < lo + seq), 0.0, NEG_INF).astype(F32)


ATTN_BLOCKS_PER_TRIP = 4


def _skewed_blocks(n_blk, produce, consume):
    per = ATTN_BLOCKS_PER_TRIP
    produce(0, 0)

    def trip(i, carry):
        for u in range(per):
            consume(per * i + u, u % 2)
            produce(per * i + u + 1, (u + 1) % 2)
        return carry

    n_trips = n_blk // per - 1
    lax.fori_loop(0, n_trips, trip, 0)
    for b in range(per * n_trips, n_blk):
        consume(b, b % 2)
        if b + 1 < n_blk:
            produce(b + 1, (b + 1) % 2)


def _block_base(b):
    return b * QBLK if isinstance(b, int) else pl.multiple_of(b * QBLK, QBLK)


def _attn_fwd(qn, kn, proj, name, rider=None):
    s = qn.shape[0]
    n_pairs = N_SLOT_HEADS * HEAD_DIM // LANES
    v_col0 = 2 * qn.shape[1] // LANES
    nt_dims = (((1,), (1,)), ((), ()))

    def body(q_ref, k_ref, v_ref, attn_ref, lse_ref, attn_b_ref, q_rm, k_rm, v_rm, acc_rm, m_rm, l_rm,
             acc_p, m_p, l_p, m_run, l_run, acc_run, band, s_buf, m_buf):
        g = pl.program_id(1)
        zpad = jnp.zeros((HALF_SPAN, LANES), BF16)
        k_rm[0:HALF_SPAN, :] = zpad
        k_rm[s + HALF_SPAN: s + 2 * HALF_SPAN, :] = zpad
        v_rm[0:HALF_SPAN, 0:LANES] = zpad
        v_rm[s + HALF_SPAN: s + 2 * HALF_SPAN, 0:LANES] = zpad

        def ones_rows(rows):
            v_rm[pl.ds(rows.start, rows.size), LANES:2 * LANES] = jnp.ones((rows.size, LANES), BF16)

        _row_chunks(s + 2 * HALF_SPAN, ones_rows, chunk=2 * HALF_SPAN)
        band[...] = _band_bias()
        lane = lax.broadcasted_iota(jnp.int32, (QBLK, LANES), 1)
        low = lane < HEAD_DIM
        n_blk = s // QBLK

        for gi, d in enumerate(DILATIONS):
            @pl.when(g == gi)
            def _(gi=gi, d=d):
                seq = s // d
                _to_residue_major(q_rm, q_ref, s, d, cast=BF16)
                _to_residue_major(k_rm, k_ref, s, d, dst_off=HALF_SPAN, cast=BF16)
                _to_residue_major(v_rm.at[:, 0:LANES], v_ref, s, d, dst_off=HALF_SPAN, cast=BF16)

                def scores(b, slot):
                    base = _block_base(b)
                    q = q_rm[pl.ds(base, QBLK), :]
                    zero = jnp.zeros_like(q)
                    q2 = jnp.concatenate([jnp.where(low, q, zero), jnp.where(low, zero, q)], axis=0)
                    sc = lax.dot_general(q2, k_rm[pl.ds(base, KWIN), :], nt_dims, preferred_element_type=F32)
                    bias = band[...] + _range_bias(base, seq)
                    for hh in range(2):
                        rows = slice(hh * QBLK, (hh + 1) * QBLK)
                        sh = sc[rows, :] + bias
                        s_buf[slot, rows, :] = sh
                        m_buf[slot, rows, :] = jnp.broadcast_to(jnp.max(sh, axis=-1, keepdims=True), (QBLK, LANES))

                def outputs(b, slot):
                    base = _block_base(b)
                    sv = s_buf[slot]
                    mb = m_buf[slot]
                    p = jnp.exp(jnp.concatenate([sv[:, 0:LANES] - mb, sv[:, LANES:2 * LANES] - mb], axis=1))
                    pv = jnp.dot(p.astype(BF16), v_rm[pl.ds(base, KWIN), :], preferred_element_type=F32)
                    rows = pl.ds(base, QBLK)
                    acc_rm[rows, :] = jnp.where(low, pv[0:QBLK, 0:LANES], pv[QBLK:2 * QBLK, 0:LANES])
                    l_rm[rows, :] = jnp.where(low, pv[0:QBLK, LANES:2 * LANES], pv[QBLK:2 * QBLK, LANES:2 * LANES])
                    m_rm[rows, :] = jnp.where(low, mb[0:QBLK, :], mb[QBLK:2 * QBLK, :])

                _skewed_blocks(n_blk, scores, outputs)
                if d == 1:
                    src = (acc_rm, m_rm, l_rm)
                else:
                    for dst_, src_ in ((acc_p, acc_rm), (m_p, m_rm), (l_p, l_rm)):
                        _from_residue_major(dst_, src_, s, d)
                    src = (acc_p, m_p, l_p)

                def combine(rows):
                    a_g, m_g, l_g = src[0][rows, :], src[1][rows, :], src[2][rows, :]
                    if gi == 0:
                        m_new, l_new, a_new = m_g, l_g, a_g
                    else:
                        m_old = m_run[rows, :]
                        m_new = jnp.maximum(m_old, m_g)
                        w_old = jnp.exp(m_old - m_new)
                        w_g = jnp.exp(m_g - m_new)
                        l_new = l_run[rows, :] * w_old + l_g * w_g
                        a_new = acc_run[rows, :] * w_old + a_g * w_g
                    if gi == len(DILATIONS) - 1:
                        out = a_new / l_new
                        attn_ref[rows, :] = out
                        attn_b_ref[rows, :] = out.astype(BF16)
                        lse_ref[rows, :] = m_new + jnp.log(l_new)
                    else:
                        m_run[rows, :] = m_new
                        l_run[rows, :] = l_new
                        acc_run[rows, :] = a_new

                _row_chunks(s, combine)

    qk_spec = pl.BlockSpec((s, LANES), lambda hp, g: (0, g * n_pairs + hp))
    v_spec = pl.BlockSpec((s, LANES), lambda hp, g: (0, v_col0 + g * n_pairs + hp))
    o_spec = pl.BlockSpec((s, LANES), lambda hp, g: (0, hp))
    f32buf = pltpu.VMEM((s, LANES), F32)
    return _pallas(
        body, name=name, grid=(n_pairs, len(DILATIONS)), in_specs=[qk_spec, qk_spec, v_spec],
        out_specs=[o_spec, o_spec, o_spec],
        out_shape=[jax.ShapeDtypeStruct((s, n_pairs * LANES), F32)] * 2
        + [jax.ShapeDtypeStruct((s, n_pairs * LANES), BF16)],
        operands=[qn, kn, proj],
        scratch_shapes=[pltpu.VMEM((s, LANES), BF16), pltpu.VMEM((s + 2 * HALF_SPAN, LANES), BF16),
                        pltpu.VMEM((s + 2 * HALF_SPAN, 2 * LANES), BF16)] + [f32buf] * 9
        + [pltpu.VMEM((QBLK, KWIN), F32), pltpu.VMEM((2, 2 * QBLK, KWIN), F32),
           pltpu.VMEM((2, 2 * QBLK, LANES), F32)],
        rider=rider)


def _attn_bwd(qn, kn, proj, dattn, attn, lse, bd, name, rider=None):
    s = qn.shape[0]
    n_pairs = N_SLOT_HEADS * HEAD_DIM // LANES
    v_col0 = 2 * qn.shape[1] // LANES
    nt_dims = (((1,), (1,)), ((), ()))
    tn_dims = (((0,), (0,)), ((), ()))
    spad = s + 2 * HALF_SPAN

    def body(q_ref, k_ref, v_ref, do_ref, o_ref, lse_ref, bd_ref, dq_ref, dk_ref, dv_ref,
             q_rm, k_rm, v_rm, do_rm, lse0_rm, lse1_rm, dd0_rm, dd1_rm, dq_rm, dk_rm, dv_rm,
             lse0_p, lse1_p, dd0_p, dd1_p, band, p_buf, ds_buf):
        g = pl.program_id(1)
        zpad = jnp.zeros((HALF_SPAN, LANES), BF16)
        for buf in (k_rm, v_rm):
            buf[0:HALF_SPAN, :] = zpad
            buf[s + HALF_SPAN: spad, :] = zpad
        zf = jnp.zeros((HALF_SPAN, LANES), F32)
        for buf in (dk_rm, dv_rm):
            buf[0:HALF_SPAN, :] = zf
            buf[s + HALF_SPAN: spad, :] = zf
        band[...] = _band_bias()

        def clear(rows):
            z = jnp.zeros((rows.size, LANES), F32)
            dk_rm[pl.ds(rows.start + HALF_SPAN, rows.size), :] = z
            dv_rm[pl.ds(rows.start + HALF_SPAN, rows.size), :] = z

        _row_chunks(s, clear)

        def prepare(rows):
            lo = lax.broadcasted_iota(jnp.int32, (rows.size, LANES), 1) < HEAD_DIM
            dsum = _head_sums(do_ref[rows, :] * o_ref[rows, :], bd_ref[...])
            dswap = pltpu.roll(dsum, HEAD_DIM, axis=1)
            dd0_p[rows, :] = jnp.where(lo, dsum, dswap)
            dd1_p[rows, :] = jnp.where(lo, dswap, dsum)
            lv = lse_ref[rows, :]
            lswap = pltpu.roll(lv, HEAD_DIM, axis=1)
            lse0_p[rows, :] = jnp.where(lo, lv, lswap)
            lse1_p[rows, :] = jnp.where(lo, lswap, lv)

        @pl.when(g == 0)
        def _():
            _row_chunks(s, prepare)
        lane = lax.broadcasted_iota(jnp.int32, (QBLK, LANES), 1)
        low = lane < HEAD_DIM
        n_blk = s // QBLK

        def stacked(ref, rows):
            val = ref[rows, :]
            zero = jnp.zeros_like(val)
            return jnp.concatenate([jnp.where(low, val, zero), jnp.where(low, zero, val)], axis=0)

        for gi, d in enumerate(DILATIONS):
            @pl.when(g == gi)
            def _(d=d):
                seq = s // d
                _to_residue_major(q_rm, q_ref, s, d, cast=BF16)
                _to_residue_major(k_rm, k_ref, s, d, dst_off=HALF_SPAN, cast=BF16)
                _to_residue_major(v_rm, v_ref, s, d, dst_off=HALF_SPAN, cast=BF16)
                _to_residue_major(do_rm, do_ref, s, d, cast=BF16)
                for dst_, src_ in ((lse0_rm, lse0_p), (lse1_rm, lse1_p), (dd0_rm, dd0_p), (dd1_rm, dd1_p)):
                    _to_residue_major(dst_, src_, s, d)

                def scores(b, slot):
                    base = _block_base(b)
                    rows = pl.ds(base, QBLK)
                    win = pl.ds(base, KWIN)
                    sc = lax.dot_general(stacked(q_rm, rows), k_rm[win, :], nt_dims, preferred_element_type=F32)
                    dp = lax.dot_general(stacked(do_rm, rows), v_rm[win, :], nt_dims, preferred_element_type=F32)
                    bias = band[...] + _range_bias(base, seq)
                    for hh, (lse_r, dd_r) in enumerate(((lse0_rm, dd0_rm), (lse1_rm, dd1_rm))):
                        r = slice(hh * QBLK, (hh + 1) * QBLK)
                        lse_h = lse_r[rows, :]
                        dd_h = dd_r[rows, :]
                        sh = sc[r, :] + bias
                        p = jnp.exp(jnp.concatenate([sh[:, 0:LANES] - lse_h, sh[:, LANES:KWIN] - lse_h], axis=1))
                        dph = dp[r, :]
                        ds = p * jnp.concatenate([dph[:, 0:LANES] - dd_h, dph[:, LANES:KWIN] - dd_h], axis=1)
                        p_buf[slot, r, :] = p.astype(BF16)
                        ds_buf[slot, r, :] = ds.astype(BF16)

                def grads(b, slot):
                    base = _block_base(b)
                    rows = pl.ds(base, QBLK)
                    win = pl.ds(base, KWIN)
                    p = p_buf[slot]
                    ds = ds_buf[slot]
                    dq2 = jnp.dot(ds, k_rm[win, :], preferred_element_type=F32)
                    dq_rm[rows, :] = jnp.where(low, dq2[0:QBLK, :], dq2[QBLK:2 * QBLK, :])
                    dk_rm[win, :] += lax.dot_general(ds, stacked(q_rm, rows), tn_dims, preferred_element_type=F32)
                    dv_rm[win, :] += lax.dot_general(p, stacked(do_rm, rows), tn_dims, preferred_element_type=F32)

                _skewed_blocks(n_blk, scores, grads)
                _from_residue_major(dq_ref, dq_rm, s, d)
                _from_residue_major(dk_ref, dk_rm, s, d, src_off=HALF_SPAN)
                _from_residue_major(dv_ref, dv_rm, s, d, src_off=HALF_SPAN)

    qk_spec = pl.BlockSpec((s, LANES), lambda hp, g: (0, g * n_pairs + hp))
    v_spec = pl.BlockSpec((s, LANES), lambda hp, g: (0, v_col0 + g * n_pairs + hp))
    o_spec = pl.BlockSpec((s, LANES), lambda hp, g: (0, hp))
    width = qn.shape[1]
    f32buf = pltpu.VMEM((s, LANES), F32)
    f32pad = pltpu.VMEM((spad, LANES), F32)
    return _pallas(
        body, name=name, grid=(n_pairs, len(DILATIONS)),
        in_specs=[qk_spec, qk_spec, v_spec, o_spec, o_spec, o_spec,
                  pl.BlockSpec((LANES, LANES), lambda hp, g: (0, 0))],
        out_specs=[qk_spec, qk_spec, qk_spec],
        out_shape=[jax.ShapeDtypeStruct((s, width), F32)] * 3,
        operands=[qn, kn, proj, dattn, attn, lse, bd],
        scratch_shapes=[pltpu.VMEM((s, LANES), BF16), pltpu.VMEM((spad, LANES), BF16),
                        pltpu.VMEM((spad, LANES), BF16), pltpu.VMEM((s, LANES), BF16),
                        f32buf, f32buf, f32buf, f32buf, f32buf, f32pad, f32pad,
                        f32buf, f32buf, f32buf, f32buf, pltpu.VMEM((QBLK, KWIN), F32),
                        pltpu.VMEM((2, 2 * QBLK, KWIN), BF16), pltpu.VMEM((2, 2 * QBLK, KWIN), BF16)],
        rider=rider)


CONV_PAD = 16


def _conv_fwd(proj, conv_w, conv_b, col0, name, rider=None):
    s = proj.shape[0]
    ch = conv_w.shape[1]
    nblk = ch // LANES
    a0 = col0 // LANES
    tr = 256
    shift = CONV_PAD - (CONV_WIDTH - 1) // 2

    def body(a_ref, b_ref, w_ref, bias_ref, u0_ref, uc_ref, pad):
        z = jnp.zeros((CONV_PAD, LANES), F32)
        pad[0:CONV_PAD, :] = z
        pad[s + CONV_PAD: s + 2 * CONV_PAD, :] = z

        def glu(rows):
            u0 = a_ref[rows, :] * jax.nn.sigmoid(b_ref[rows, :])
            u0_ref[rows, :] = u0
            pad[pl.ds(rows.start + CONV_PAD, rows.size), :] = u0

        _row_chunks(s, glu)
        for t in range(0, s, tr):
            acc = jnp.broadcast_to(bias_ref[...], (tr, LANES))
            for k in range(CONV_WIDTH):
                acc = acc + w_ref[k:k + 1, :] * pad[t + k + shift: t + k + shift + tr, :]
            uc_ref[t:t + tr, :] = acc

    return _pallas(
        body, name=name, grid=(nblk,),
        in_specs=[pl.BlockSpec((s, LANES), lambda c: (0, a0 + c)),
                  pl.BlockSpec((s, LANES), lambda c: (0, a0 + nblk + c)),
                  pl.BlockSpec((CONV_WIDTH, LANES), lambda c: (0, c)),
                  pl.BlockSpec((1, LANES), lambda c: (0, c))],
        out_specs=[pl.BlockSpec((s, LANES), lambda c: (0, c))] * 2,
        out_shape=[jax.ShapeDtypeStruct((s, ch), F32)] * 2, operands=[proj, proj, conv_w, conv_b],
        scratch_shapes=[pltpu.VMEM((s + 2 * CONV_PAD, LANES), F32)], rider=rider)


def _ln_silu_fwd(uc, ln_w, ln_b, name):
    s, ch = uc.shape
    tm = 256

    def body(u_ref, w_ref, b_ref, o_ref):
        u = u_ref[...]
        mu = jnp.mean(u, axis=-1, keepdims=True)
        xc = u - mu
        rstd = lax.rsqrt(jnp.mean(xc * xc, axis=-1, keepdims=True) + EPS)
        z = xc * rstd * w_ref[...] + b_ref[...]
        o_ref[...] = (z * jax.nn.sigmoid(z)).astype(BF16)

    row = pl.BlockSpec((tm, ch), lambda i: (i, 0))
    vec = pl.BlockSpec((1, ch), lambda i: (0, 0))
    return pl.pallas_call(
        body, name=name, grid=(s // tm,), in_specs=[row, vec, vec], out_specs=row,
        out_shape=jax.ShapeDtypeStruct((s, ch), BF16), compiler_params=_params(),
    )(uc, ln_w, ln_b)


def _ln_silu_bwd(du3, uc, ln_w, ln_b, name):
    s, ch = uc.shape
    tm = 256

    def body(d_ref, u_ref, w_ref, b_ref, du_ref, dw_ref, db_ref):
        u = u_ref[...]
        mu = jnp.mean(u, axis=-1, keepdims=True)
        xc = u - mu
        rstd = lax.rsqrt(jnp.mean(xc * xc, axis=-1, keepdims=True) + EPS)
        xhat = xc * rstd
        z = xhat * w_ref[...] + b_ref[...]
        sg = jax.nn.sigmoid(z)
        dz = d_ref[...] * (sg * (1.0 + z * (1.0 - sg)))
        dxh = dz * w_ref[...]
        du_ref[...] = rstd * (dxh - jnp.mean(dxh, axis=-1, keepdims=True)
                              - xhat * jnp.mean(dxh * xhat, axis=-1, keepdims=True))
        pw = jnp.sum(dz * xhat, axis=0, keepdims=True)
        pb = jnp.sum(dz, axis=0, keepdims=True)
        first = pl.program_id(0) == 0

        @pl.when(first)
        def _():
            dw_ref[...] = pw
            db_ref[...] = pb

        @pl.when(jnp.logical_not(first))
        def _():
            dw_ref[...] += pw
            db_ref[...] += pb

    row = pl.BlockSpec((tm, ch), lambda i: (i, 0))
    vec = pl.BlockSpec((1, ch), lambda i: (0, 0))
    return pl.pallas_call(
        body, name=name, grid=(s // tm,), in_specs=[row, row, vec, vec], out_specs=[row, vec, vec],
        out_shape=[jax.ShapeDtypeStruct((s, ch), F32), jax.ShapeDtypeStruct((1, ch), F32),
                   jax.ShapeDtypeStruct((1, ch), F32)],
        compiler_params=_params(),
    )(du3, uc, ln_w, ln_b)


def _conv_bwd(duc, u0, proj, conv_w, col0, name, rider=None):
    s = proj.shape[0]
    ch = conv_w.shape[1]
    nblk = ch // LANES
    a0 = col0 // LANES
    tr = 256
    half = (CONV_WIDTH - 1) // 2
    shift = CONV_PAD - half

    def body(duc_ref, u0_ref, a_ref, b_ref, w_ref, da_ref, db_ref, dw_ref, dbias_ref, pad_d, pad_u):
        z = jnp.zeros((CONV_PAD, LANES), F32)
        for buf in (pad_d, pad_u):
            buf[0:CONV_PAD, :] = z
            buf[s + CONV_PAD: s + 2 * CONV_PAD, :] = z

        def fill(rows):
            dst = pl.ds(rows.start + CONV_PAD, rows.size)
            pad_d[dst, :] = duc_ref[rows, :]
            pad_u[dst, :] = u0_ref[rows, :]

        _row_chunks(s, fill)
        dw_acc = [jnp.zeros((8, LANES), F32) for _ in range(CONV_WIDTH)]
        dbias_acc = jnp.zeros((8, LANES), F32)
        for t in range(0, s, tr):
            d_t = duc_ref[t:t + tr, :]
            dbias_acc = dbias_acc + jnp.sum(d_t.reshape(tr // 8, 8, LANES), axis=0)
            du0 = jnp.zeros((tr, LANES), F32)
            for k in range(CONV_WIDTH):
                du0 = du0 + w_ref[k:k + 1, :] * pad_d[t - k + half + CONV_PAD: t - k + half + CONV_PAD + tr, :]
                prod = d_t * pad_u[t + k + shift: t + k + shift + tr, :]
                dw_acc[k] = dw_acc[k] + jnp.sum(prod.reshape(tr // 8, 8, LANES), axis=0)
            av = a_ref[t:t + tr, :]
            sg = jax.nn.sigmoid(b_ref[t:t + tr, :])
            da_ref[t:t + tr, :] = (du0 * sg).astype(BF16)
            db_ref[t:t + tr, :] = (du0 * av * sg * (1.0 - sg)).astype(BF16)
        for k in range(CONV_WIDTH):
            dw_ref[k:k + 1, :] = jnp.sum(dw_acc[k], axis=0, keepdims=True)
        dbias_ref[...] = jnp.sum(dbias_acc, axis=0, keepdims=True)

    col = lambda off: pl.BlockSpec((s, LANES), lambda c: (0, off + c))
    return _pallas(
        body, name=name, grid=(nblk,),
        in_specs=[col(0), col(0), col(a0), col(a0 + nblk),
                  pl.BlockSpec((CONV_WIDTH, LANES), lambda c: (0, c))],
        out_specs=[col(0), col(0), pl.BlockSpec((CONV_WIDTH, LANES), lambda c: (0, c)),
                   pl.BlockSpec((1, LANES), lambda c: (0, c))],
        out_shape=[jax.ShapeDtypeStruct((s, ch), BF16)] * 2
        + [jax.ShapeDtypeStruct((CONV_WIDTH, ch), F32), jax.ShapeDtypeStruct((1, ch), F32)],
        operands=[duc, u0, proj, proj, conv_w],
        scratch_shapes=[pltpu.VMEM((s + 2 * CONV_PAD, LANES), F32)] * 2, rider=rider)


GATE_BLK = 512


def _gate_fwd(proj, bg, y_a, y_b, col0, name):
    s, d = y_a.shape
    tm = 256
    g0 = col0 // GATE_BLK
    nb = d // GATE_BLK

    def body(ga_ref, gb_ref, ba_ref, bb_ref, ya_ref, yb_ref, o_ref):
        g_a = jax.nn.sigmoid(ga_ref[...] + ba_ref[...])
        g_b = jax.nn.sigmoid(gb_ref[...] + bb_ref[...])
        o_ref[...] = (g_a * ya_ref[...] + g_b * yb_ref[...]).astype(BF16)

    act = pl.BlockSpec((tm, GATE_BLK), lambda i, j: (i, j))
    return pl.pallas_call(
        body, name=name, grid=(s // tm, nb),
        in_specs=[pl.BlockSpec((tm, GATE_BLK), lambda i, j: (i, g0 + j)),
                  pl.BlockSpec((tm, GATE_BLK), lambda i, j: (i, g0 + nb + j)),
                  pl.BlockSpec((None, 1, GATE_BLK), lambda i, j: (0, 0, j)),
                  pl.BlockSpec((None, 1, GATE_BLK), lambda i, j: (1, 0, j)), act, act],
        out_specs=act, out_shape=jax.ShapeDtypeStruct((s, d), BF16), compiler_params=_params(),
    )(proj, proj, bg, bg, y_a, y_b)


def _gate_bwd(d_mixed, proj, bg, y_a, y_b, col0, name, rider=None):
    s, d = y_a.shape
    tm = 256
    half = d // 2
    assert col0 % half == 0
    c0 = col0 // half

    def body(dm_ref, a0_ref, a1_ref, b0_ref, b1_ref, bias_ref, ya_ref, yb_ref, dgl_ref, dya_ref, dyb_ref, db_ref):
        dm = dm_ref[...]
        parts = []
        for br, (lo_ref, hi_ref, y_ref, dy_ref) in enumerate(((a0_ref, a1_ref, ya_ref, dya_ref),
                                                              (b0_ref, b1_ref, yb_ref, dyb_ref))):
            logits = jnp.concatenate([lo_ref[...], hi_ref[...]], axis=1)
            gate = jax.nn.sigmoid(logits + bias_ref[br])
            dy_ref[...] = (dm * gate).astype(BF16)
            dgl = dm * y_ref[...] * gate * (1.0 - gate)
            dgl_ref[:, br * d:(br + 1) * d] = dgl.astype(BF16)
            parts.append(jnp.sum(dgl, axis=0, keepdims=True))
        part = jnp.concatenate(parts, axis=0)
        first = pl.program_id(0) == 0

        @pl.when(first)
        def _():
            db_ref[...] = part

        @pl.when(jnp.logical_not(first))
        def _():
            db_ref[...] += part

    row = pl.BlockSpec((tm, d), lambda i: (i, 0))
    logit_blk = lambda k: pl.BlockSpec((tm, half), functools.partial(lambda i, k: (i, c0 + k), k=k))
    return _pallas(
        body, name=name, grid=(s // tm,),
        in_specs=[row, logit_blk(0), logit_blk(1), logit_blk(2), logit_blk(3),
                  pl.BlockSpec((2, 1, d), lambda i: (0, 0, 0)), row, row],
        out_specs=[pl.BlockSpec((tm, 2 * d), lambda i: (i, 0)), row, row, pl.BlockSpec((2, d), lambda i: (0, 0))],
        out_shape=[jax.ShapeDtypeStruct((s, 2 * d), BF16), jax.ShapeDtypeStruct((s, d), BF16),
                   jax.ShapeDtypeStruct((s, d), BF16), jax.ShapeDtypeStruct((2, d), F32)],
        operands=[d_mixed, proj, proj, proj, proj, bg, y_a, y_b], rider=rider)


def _ffn_in_swiglu(h2, w_blocked, name):
    s, k = h2.shape
    nblk, _, tn = w_blocked.shape
    ff = nblk // 2 * tn
    tm = 512

    def body(a_ref, wg_ref, wu_ref, g_ref, u_ref, act_ref):
        a = a_ref[...]
        gt = jnp.dot(a, wg_ref[...], preferred_element_type=F32)
        up = jnp.dot(a, wu_ref[...], preferred_element_type=F32)
        g_ref[...] = gt
        u_ref[...] = up
        act_ref[...] = (gt * jax.nn.sigmoid(gt) * up).astype(BF16)

    out = pl.BlockSpec((tm, tn), lambda j, i: (i, j))
    return pl.pallas_call(
        body, name=name, grid=(nblk // 2, s // tm),
        in_specs=[pl.BlockSpec((tm, k), lambda j, i: (i, 0)),
                  pl.BlockSpec((None, k, tn), lambda j, i: (j, 0, 0)),
                  pl.BlockSpec((None, k, tn), lambda j, i: (nblk // 2 + j, 0, 0))],
        out_specs=[out, out, out],
        out_shape=[jax.ShapeDtypeStruct((s, ff), F32), jax.ShapeDtypeStruct((s, ff), F32),
                   jax.ShapeDtypeStruct((s, ff), BF16)],
        compiler_params=_params(),
    )(h2, w_blocked, w_blocked)


def _swiglu_bwd(gate, up, d_act, name, rider=None):
    s, ff = gate.shape
    tm = 256

    def body(g_ref, u_ref, d_ref, o_ref):
        gt = g_ref[...]
        sg = jax.nn.sigmoid(gt)
        dv = d_ref[...]
        o_ref[:, 0:ff] = (dv * u_ref[...] * (sg * (1.0 + gt * (1.0 - sg)))).astype(BF16)
        o_ref[:, ff:2 * ff] = (dv * gt * sg).astype(BF16)

    row = pl.BlockSpec((tm, ff), lambda i: (i, 0))
    return _pallas(
        body, name=name, grid=(s // tm,), in_specs=[row, row, row],
        out_specs=pl.BlockSpec((tm, 2 * ff), lambda i: (i, 0)),
        out_shape=jax.ShapeDtypeStruct((s, 2 * ff), BF16), operands=[gate, up, d_act], rider=rider)


def _out_proj_rmsnorm(mixed, w_out, x, norm_w, name):
    s, k = mixed.shape
    d = w_out.shape[1]
    tm = 512

    def body(a_ref, w_ref, x_ref, nw_ref, x1_ref, h2_ref):
        x1 = x_ref[...] + jnp.dot(a_ref[...], w_ref[...], preferred_element_type=F32)
        x1_ref[...] = x1
        rstd = lax.rsqrt(jnp.mean(x1 * x1, axis=-1, keepdims=True) + EPS)
        h2_ref[...] = (x1 * rstd * nw_ref[...]).astype(BF16)

    row = pl.BlockSpec((tm, d), lambda i: (i, 0))
    return pl.pallas_call(
        body, name=name, grid=(s // tm,),
        in_specs=[pl.BlockSpec((tm, k), lambda i: (i, 0)), pl.BlockSpec((k, d), lambda i: (0, 0)), row,
                  pl.BlockSpec((1, d), lambda i: (0, 0))],
        out_specs=[row, row],
        out_shape=[jax.ShapeDtypeStruct((s, d), F32), jax.ShapeDtypeStruct((s, d), BF16)],
        compiler_params=_params(),
    )(mixed, w_out, x, norm_w)


def _ffn_out_loss(act, w_ffn_out, x1, target, name):
    s, k = act.shape
    d = w_ffn_out.shape[1]
    tm = 512

    def body(a_ref, w_ref, x1_ref, t_ref, dy_ref, dyb_ref, loss_ref, acc):
        y = x1_ref[...] + jnp.dot(a_ref[...], w_ref[...], preferred_element_type=F32)
        diff = y - t_ref[...]
        dy = diff * (1.0 / d)
        dy_ref[...] = dy
        dyb_ref[...] = dy.astype(BF16)
        part = jnp.sum((diff * diff).reshape(tm // 8, 8, d), axis=0)
        i = pl.program_id(0)

        @pl.when(i == 0)
        def _():
            acc[...] = part

        @pl.when(i > 0)
        def _():
            acc[...] += part

        @pl.when(i == pl.num_programs(0) - 1)
        def _():
            loss_ref[...] = (0.5 / d) * jnp.sum(jnp.sum(acc[...], axis=1, keepdims=True), axis=0, keepdims=True)

    row = pl.BlockSpec((tm, d), lambda i: (i, 0))
    return pl.pallas_call(
        body, name=name, grid=(s // tm,),
        in_specs=[pl.BlockSpec((tm, k), lambda i: (i, 0)), pl.BlockSpec((k, d), lambda i: (0, 0)), row, row],
        out_specs=[row, row, pl.BlockSpec((1, 1), lambda i: (0, 0))],
        out_shape=[jax.ShapeDtypeStruct((s, d), F32), jax.ShapeDtypeStruct((s, d), BF16),
                   jax.ShapeDtypeStruct((1, 1), F32)],
        scratch_shapes=[pltpu.VMEM((8, d), F32)], compiler_params=_params(),
    )(act, w_ffn_out, x1, target)


LATE_GATHER = ("w_o_attn", "w_pw_conv", "w_out", "w_ffn_in", "w_ffn_out")
EARLY_REDUCE = LATE_GATHER


def _blocks_by_half(g):
    if g.ndim == 2:
        g = g.reshape(N_CHIPS, g.shape[0] // N_CHIPS, g.shape[1])
    return g.reshape(N_CHIPS, 2, g.shape[1] // 2, g.shape[2])


def _forward_backward(x, pos_col, target, wts, late_bufs, pos_arr):
    wts = dict(wts)
    consts = _rope_consts()
    bd = consts[3]
    qw2 = jnp.tile(wts["q_norm_w"], (1, LANES // HEAD_DIM))
    kw2 = jnp.tile(wts["k_norm_w"], (1, LANES // HEAD_DIM))
    qkv_w = 3 * N_SLOT_HEADS * HEAD_DIM
    conv_col0 = 3 * qkv_w
    ch = wts["conv_w"].shape[1]
    gate_col0 = conv_col0 + 2 * ch

    h = _rmsnorm_fwd(x, wts["norm1_w"], "rms1_fwd")
    gathering, started = _split_start(_gather_ici_rider(late_bufs, []), "late_gather_start", after=[wts["w_in"]])
    proj = _matmul(h, wts["w_in"], mode="nn", tm=512, tn=1920, tk=1024, out_dtype=F32, name="mm_proj",
                   b_blocked=True, cols_outer=True, after=[started])
    qn, kn = _qk_fwd(proj, pos_col, qw2, kw2, consts, "qk_fwd")
    attn, lse, attn_b = _attn_fwd(qn, kn, proj, "attn_fwd")
    late_bufs, _ = _split_wait(gathering, after=[attn_b], name="late_gather_wait")
    (u0, uc), late_bufs = _conv_fwd(proj, wts["conv_w"], wts["conv_b"], conv_col0, "conv_fwd",
                                    rider=_gather_forward_rider(late_bufs))
    for n, buf in zip(LATE_GATHER, late_bufs):
        full = buf.reshape(N_CHIPS, -1, buf.shape[3])
        wts[n] = full.reshape(-1, full.shape[2]) if n in ROW_SHARDED else full
    y_a = _matmul(attn_b, wts["w_o_attn"], mode="nn", tm=1024, tn=256, tk=512, out_dtype=F32, name="mm_ya",
                  b_blocked=True)
    u3 = _ln_silu_fwd(uc, wts["conv_ln_w"], wts["conv_ln_b"], "ln_fwd")
    y_b = _matmul(u3, wts["w_pw_conv"], mode="nn", tm=1024, tn=256, tk=512, out_dtype=F32, name="mm_yb",
                  b_blocked=True)
    mixed = _gate_fwd(proj, wts["b_gate"], y_a, y_b, gate_col0, "gate_fwd")
    x1, h2 = _out_proj_rmsnorm(mixed, wts["w_out"], x, wts["norm2_w"], "mm_x1_rms2")
    gate, up, act = _ffn_in_swiglu(h2, wts["w_ffn_in"], "mm_gu_swiglu")
    dy, dy_b16, loss = _ffn_out_loss(act, wts["w_ffn_out"], x1, target, "mm_x2_loss")

    g = {}
    by_chip = {}

    def pair_add(n, blocks, received):
        return _add_own_half(blocks, received, pos_arr, f"grads_pair_add_{n}")

    d_act = _matmul(dy_b16, wts["w_ffn_out"], mode="nt", tm=512, tn=1408, tk=1024, out_dtype=F32, name="mm_dact",
                    cols_outer=True)
    g_ffn_out = _blocks_by_half(
        _matmul(act, dy_b16, mode="tn", tm=1408, tn=1024, tk=2048, out_dtype=F32, name="mm_dwffnout"))
    dgu, (received,) = _swiglu_bwd(gate, up, d_act, "swiglu_bwd",
                                   rider=_pair_exchange_rider([g_ffn_out], halved=True))
    to_send, own = pair_add("w_ffn_out", g_ffn_out, received)
    dh2, (by_chip["w_ffn_out"],) = _matmul(
        dgu, wts["w_ffn_in"], mode="nt", tm=1024, tn=1024, tk=1408, out_dtype=F32, name="mm_dh2", b_blocked=True,
        rider=_chip_exchange_rider([to_send], [own]))
    g_ffn_in = _blocks_by_half(_matmul(h2, dgu, mode="tn", tm=512, tn=1408, tk=2048, out_dtype=F32,
                                       name="mm_dwffnin", out_blocked=N_CHIPS, cols_outer=True))
    exchanging, started = _split_start(_pair_exchange_rider([g_ffn_in], halved=True), "grads_ffn_in_pair_start")
    dx1, dx1_b16, g["norm2_w"] = _rmsnorm_bwd(dh2, x1, wts["norm2_w"], dy, "rms2_bwd")
    d_mixed = _matmul(dx1_b16, wts["w_out"], mode="nt", tm=512, tn=1024, tk=1024, out_dtype=F32, name="mm_dmixed",
                      after=[started])
    g["w_out"] = _matmul(mixed, dx1_b16, mode="tn", tm=512, tn=1024, tk=2048, out_dtype=F32, name="mm_dwout")
    dgl, dy_a, dy_b, g["b_gate"] = _gate_bwd(d_mixed, proj, wts["b_gate"], y_a, y_b, gate_col0, "gate_bwd")
    (received,), (g_ffn_in,) = _split_wait(exchanging, after=[dgl], name="grads_ffn_in_pair_wait")
    ffn_in_to_send, ffn_in_own = pair_add("w_ffn_in", g_ffn_in, received)
    dattn = _matmul(dy_a, wts["w_o_attn"], mode="nt", tm=1024, tn=512, tk=256, out_dtype=F32, name="mm_dattn",
                    b_blocked=True)
    g["w_o_attn"] = _matmul(attn_b, dy_a, mode="tn", tm=512, tn=256, tk=2048, out_dtype=F32, name="mm_dwo",
                            out_blocked=N_CHIPS)
    du3 = _matmul(dy_b, wts["w_pw_conv"], mode="nt", tm=1024, tn=512, tk=256, out_dtype=F32, name="mm_du3",
                  b_blocked=True)
    g["w_pw_conv"] = _matmul(u3, dy_b, mode="tn", tm=512, tn=256, tk=2048, out_dtype=F32, name="mm_dwpw",
                             out_blocked=N_CHIPS)
    duc, g["conv_ln_w"], g["conv_ln_b"] = _ln_silu_bwd(du3, uc, wts["conv_ln_w"], wts["conv_ln_b"], "ln_bwd")

    small3 = ("w_out", "w_o_attn", "w_pw_conv")
    g_small3 = [_blocks_by_half(g.pop(n)) for n in small3]
    (da, db, g["conv_w"], g["conv_b"]), received = _conv_bwd(
        duc, u0, proj, wts["conv_w"], conv_col0, "conv_bwd", rider=_pair_exchange_rider(g_small3, halved=True))
    sums3 = [pair_add(n, gb, rv) for n, gb, rv in zip(small3, g_small3, received)]
    (dqn, dkn, dv), (by_chip["w_ffn_in"],) = _attn_bwd(
        qn, kn, proj, dattn, attn, lse, bd, "attn_bwd",
        rider=_chip_exchange_rider([ffn_in_to_send], [ffn_in_own]))
    (dproj, dqw, dkw), exchanged3 = _qk_bwd(
        dqn, dkn, dv, da, db, dgl, proj, pos_col, qw2, kw2, consts, "qk_bwd",
        rider=_chip_exchange_rider([s[0] for s in sums3], [s[1] for s in sums3]))
    by_chip.update(zip(small3, exchanged3))
    halves = [_sum_chips(by_chip[n], pos_arr, f"grads_chip_sum_{n}") for n in EARLY_REDUCE]
    g["q_norm_w"] = dqw[:, :HEAD_DIM]
    g["k_norm_w"] = dkw[:, :HEAD_DIM]

    c = pos_arr[0]
    rh = h.shape[1] // 2
    h_sibling = lax.dynamic_slice_in_dim(h, (1 - c) * rh, rh, axis=1)
    h_own = lax.dynamic_slice_in_dim(h, c * rh, rh, axis=1)
    g_sibling, shards = _matmul(h_sibling, dproj, mode="tn", tm=rh, tn=1920, tk=2048, out_dtype=F32,
                                name="mm_dwin_sibling", out_blocked=N_CHIPS, rider=_pair_gather_rider(halves))
    reduced = dict(zip(EARLY_REDUCE, shards))
    exchanging, started = _split_start(_pair_exchange_rider([g_sibling], halved=False), "grads_w_in_pair_start")
    g_own = _matmul(h_own, dproj, mode="tn", tm=rh, tn=1920, tk=2048, out_dtype=F32, name="mm_dwin_own",
                    out_blocked=N_CHIPS, after=[started])
    (from_sibling,), _ = _split_wait(exchanging, after=[g_own], name="grads_w_in_pair_wait")
    to_send, own = _add_own_half(g_own, from_sibling, pos_arr, "grads_pair_add_w_in")
    in_flight, started = _split_start(_chip_exchange_rider([to_send], [own]), "grads_w_in_exchange_start")
    dh = _matmul(dproj, wts["w_in"], mode="nt", tm=1024, tn=1024, tk=1920, out_dtype=F32, name="mm_dh",
                 b_blocked=True, after=[started])
    grad_x, _, g["norm1_w"] = _rmsnorm_bwd(dh, x, wts["norm1_w"], dx1, "rms1_bwd")
    return loss, grad_x, g, reduced, (in_flight, started)


def _mesh_pos():
    return lax.axis_index("x"), lax.axis_index("y"), lax.axis_index("c")


def _other_chips(x, y):
    return [(1 - x, y), (x, 1 - y), (1 - x, 1 - y)]


def _cast_into_slot(shard, chip_arr, dtype, name):
    r, c = shard.shape
    tr = r // 2 if r % 32 == 0 else r

    def body(chip_ref, s_ref, o_ref):
        del chip_ref
        o_ref[...] = s_ref[...].astype(dtype)

    return pl.pallas_call(
        body, name=name,
        grid_spec=pltpu.PrefetchScalarGridSpec(
            num_scalar_prefetch=1, grid=(r // tr,),
            in_specs=[pl.BlockSpec((tr, c), lambda i, chip_ref: (i, 0))],
            out_specs=pl.BlockSpec((None, tr, c), lambda i, chip_ref: (chip_ref[0], i, 0))),
        out_shape=jax.ShapeDtypeStruct((N_CHIPS, r, c), dtype), compiler_params=_params(),
    )(chip_arr, shard)


def _gather_both_legs_rider(big, small):
    nb = len(big)
    n = nb + len(small)

    def part(bufs, a, slot, half):
        return bufs[a].at[slot, half] if a < nb else bufs[a].at[slot]

    def start(r_in, bufs, sems):
        x, y, c = _mesh_pos()
        me = 2 * x + y
        chips = _other_chips(x, y)
        for a in range(n):
            for k in range(3):
                px, py = chips[k]
                pltpu.make_async_remote_copy(
                    src_ref=part(bufs, a, me, c), dst_ref=part(bufs, a, me, c), send_sem=sems[0].at[a, k],
                    recv_sem=sems[1].at[a, k], device_id=(px, py, c), device_id_type=MESH).start()

    def wait(r_in, bufs, sems):
        x, y, c = _mesh_pos()
        me = 2 * x + y
        chips = _other_chips(x, y)
        sibling = (x, y, 1 - c)
        forwards = []
        for a in range(n):
            for k in range(3):
                px, py = chips[k]
                slot = 2 * px + py
                pltpu.make_async_remote_copy(
                    src_ref=part(bufs, a, slot, c), dst_ref=part(bufs, a, slot, c), send_sem=sems[0].at[a, k],
                    recv_sem=sems[1].at[a, k], device_id=(px, py, c), device_id_type=MESH).wait_recv()
                if a < nb:
                    fwd = pltpu.make_async_remote_copy(
                        src_ref=part(bufs, a, slot, c), dst_ref=part(bufs, a, slot, c), send_sem=sems[2].at[a, k],
                        recv_sem=sems[3].at[a, k], device_id=sibling, device_id_type=MESH)
                    fwd.start()
                    forwards.append(fwd)
        for a in range(nb):
            for k in range(3):
                px, py = chips[k]
                slot = 2 * px + py
                pltpu.make_async_remote_copy(
                    src_ref=part(bufs, a, slot, 1 - c), dst_ref=part(bufs, a, slot, 1 - c),
                    send_sem=sems[2].at[a, k], recv_sem=sems[3].at[a, k], device_id=sibling,
                    device_id_type=MESH).wait_recv()
        for a in range(n):
            for k in range(3):
                px, py = chips[k]
                pltpu.make_async_remote_copy(
                    src_ref=part(bufs, a, me, c), dst_ref=part(bufs, a, me, c), send_sem=sems[0].at[a, k],
                    recv_sem=sems[1].at[a, k], device_id=(px, py, c), device_id_type=MESH).wait_send()
        for fwd in forwards:
            fwd.wait_send()

    ops = list(big) + list(small)
    return _Rider(ops, [jax.ShapeDtypeStruct(o.shape, o.dtype) for o in ops], {i: i for i in range(n)},
                  [pltpu.SemaphoreType.DMA((n, 3)), pltpu.SemaphoreType.DMA((n, 3)),
                   pltpu.SemaphoreType.DMA((nb, 3)), pltpu.SemaphoreType.DMA((nb, 3))], start, wait)


def _comm_call(rider, name):
    def body():
        pass

    return _pallas(body, name=name, grid=(1,), in_specs=[], out_specs=[], out_shape=[], operands=[],
                   rider=rider)[1]


def _gather_ici_rider(big, small):
    nb = len(big)
    n = nb + len(small)

    def copies(bufs, sems):
        x, y, c = _mesh_pos()
        me = 2 * x + y
        part = lambda a, slot: bufs[a].at[slot, c] if a < nb else bufs[a].at[slot]
        out = []
        for a in range(n):
            for k, (px, py) in enumerate(_other_chips(x, y)):
                send = functools.partial(
                    pltpu.make_async_remote_copy,
                    src_ref=part(a, me), dst_ref=part(a, me), send_sem=sems[0].at[a, k],
                    recv_sem=sems[1].at[a, k], device_id=(px, py, c), device_id_type=MESH)
                recv = functools.partial(
                    pltpu.make_async_remote_copy,
                    src_ref=part(a, 2 * px + py), dst_ref=part(a, 2 * px + py), send_sem=sems[0].at[a, k],
                    recv_sem=sems[1].at[a, k], device_id=(px, py, c), device_id_type=MESH)
                out.append((send, recv))
        return out

    def start(r_in, r_out, sems):
        for send, _ in copies(r_out, sems):
            send().start()

    def wait(r_in, r_out, sems):
        cps = copies(r_out, sems)
        for _, recv in cps:
            recv().wait_recv()
        for send, _ in cps:
            send().wait_send()

    ops = list(big) + list(small)
    return _Rider(ops, [jax.ShapeDtypeStruct(o.shape, o.dtype) for o in ops], {i: i for i in range(n)},
                  [pltpu.SemaphoreType.DMA((n, 3)), pltpu.SemaphoreType.DMA((n, 3))], start, wait)


def _gather_forward_rider(big):
    n = len(big)

    def copies(bufs, sems):
        x, y, c = _mesh_pos()
        out = []
        for a in range(n):
            for k, (px, py) in enumerate(_other_chips(x, y)):
                slot = 2 * px + py
                send = functools.partial(
                    pltpu.make_async_remote_copy,
                    src_ref=bufs[a].at[slot, c], dst_ref=bufs[a].at[slot, c], send_sem=sems[0].at[a, k],
                    recv_sem=sems[1].at[a, k], device_id=(x, y, 1 - c), device_id_type=MESH)
                recv = functools.partial(
                    pltpu.make_async_remote_copy,
                    src_ref=bufs[a].at[slot, 1 - c], dst_ref=bufs[a].at[slot, 1 - c], send_sem=sems[0].at[a, k],
                    recv_sem=sems[1].at[a, k], device_id=(x, y, 1 - c), device_id_type=MESH)
                out.append((send, recv))
        return out

    def start(r_in, r_out, sems):
        for send, _ in copies(r_out, sems):
            send().start()

    def wait(r_in, r_out, sems):
        cps = copies(r_out, sems)
        for _, recv in cps:
            recv().wait_recv()
        for send, _ in cps:
            send().wait_send()

    return _Rider(big, [jax.ShapeDtypeStruct(o.shape, o.dtype) for o in big], {i: i for i in range(n)},
                  [pltpu.SemaphoreType.DMA((n, 3)), pltpu.SemaphoreType.DMA((n, 3))], start, wait)


def _pair_exchange_rider(gs, halved):
    n = len(gs)

    def copies(r_in, r_out, sems):
        x, y, c = _mesh_pos()
        return [pltpu.make_async_remote_copy(
            src_ref=r_in[a].at[:, 1 - c] if halved else r_in[a], dst_ref=r_out[a], send_sem=sems[0].at[a],
            recv_sem=sems[1].at[a], device_id=(x, y, 1 - c), device_id_type=MESH) for a in range(n)]

    def start(r_in, r_out, sems):
        for cp in copies(r_in, r_out, sems):
            cp.start()

    def wait(r_in, r_out, sems):
        for cp in copies(r_in, r_out, sems):
            cp.wait()

    return _Rider(gs, [jax.ShapeDtypeStruct((g.shape[0],) + g.shape[-2:], g.dtype) for g in gs], {},
                  [pltpu.SemaphoreType.DMA((n,)), pltpu.SemaphoreType.DMA((n,))], start, wait)


def _chip_exchange_rider(to_send, by_chip, row_range=None):
    n = len(to_send)

    def copies(r_in, r_out, sems):
        x, y, c = _mesh_pos()
        me = 2 * x + y
        rows = (lambda ref: ref) if row_range is None else (lambda ref: ref.at[pl.ds(*row_range)])
        out = []
        for a in range(n):
            for k, (px, py) in enumerate(_other_chips(x, y)):
                send = functools.partial(
                    pltpu.make_async_remote_copy,
                    src_ref=rows(r_in[a].at[2 * px + py]), dst_ref=rows(r_out[a].at[me]),
                    send_sem=sems[0].at[a, k], recv_sem=sems[1].at[a, k], device_id=(px, py, c),
                    device_id_type=MESH)
                recv = functools.partial(
                    pltpu.make_async_remote_copy,
                    src_ref=rows(r_in[a].at[me]), dst_ref=rows(r_out[a].at[2 * px + py]),
                    send_sem=sems[0].at[a, k], recv_sem=sems[1].at[a, k], device_id=(px, py, c),
                    device_id_type=MESH)
                out.append((send, recv))
        return out

    def start(r_in, r_out, sems):
        for send, _ in copies(r_in, r_out, sems):
            send().start()

    def wait(r_in, r_out, sems):
        cps = copies(r_in, r_out, sems)
        for _, recv in cps:
            recv().wait_recv()
        for send, _ in cps:
            send().wait_send()

    return _Rider(list(to_send) + list(by_chip), [jax.ShapeDtypeStruct(b.shape, b.dtype) for b in by_chip],
                  {n + i: i for i in range(n)},
                  [pltpu.SemaphoreType.DMA((n, 3)), pltpu.SemaphoreType.DMA((n, 3))], start, wait)


HBM = pl.BlockSpec(memory_space=pltpu.HBM)
SEM = pl.BlockSpec(memory_space=pltpu.SEMAPHORE)


_IN_FLIGHT = pltpu.CompilerParams(has_side_effects=pltpu.SideEffectType.DATAFLOW_SIDE_EFFECTING)


class _FlatSems:
    def __init__(self, ref, shape):
        self.ref, self.shape = ref, shape

    @property
    def at(self):
        return self

    def __getitem__(self, idx):
        idx = idx if isinstance(idx, tuple) else (idx,)
        flat = 0
        for i, n in zip(idx, self.shape):
            flat = flat * n + i
        return self.ref.at[flat]


def _flat_sem_types(rider):
    return tuple(pltpu.SemaphoreType.DMA((int(np.prod(s.shape)),)) for s in rider.scratch)


def _as_rider_sems(rider, refs):
    return [_FlatSems(r, s.shape) for r, s in zip(refs, rider.scratch)]


def _split_start(rider, name, after=()):
    n_in, n_out, n_sem = len(rider.operands), len(rider.out_shapes), len(rider.scratch)
    n_after = len(after)
    fresh = [j for j in range(n_out) if j not in rider.aliases.values()]
    by_out = {j: i for i, j in rider.aliases.items()}

    def body(*refs):
        r_in = refs[:n_in]
        refs = refs[n_in + n_after:]
        sems = refs[:n_sem]
        thru = refs[n_sem:n_sem + n_in]
        fresh_refs = refs[n_sem + n_in:n_sem + n_in + len(fresh)]
        token = refs[-1]
        r_out = [thru[by_out[j]] if j in by_out else fresh_refs[fresh.index(j)] for j in range(n_out)]
        rider.start(r_in, r_out, _as_rider_sems(rider, sems))
        token[...] = jnp.zeros_like(token)

    res = pl.pallas_call(
        body, name=name,
        out_shape=_flat_sem_types(rider) + tuple(pltpu.HBM(o.shape, o.dtype) for o in rider.operands)
        + tuple(pltpu.HBM(rider.out_shapes[j].shape, rider.out_shapes[j].dtype) for j in fresh)
        + (jax.ShapeDtypeStruct((8, LANES), F32),),
        in_specs=(HBM,) * n_in + (ANY,) * n_after,
        out_specs=(SEM,) * n_sem + (HBM,) * (n_in + len(fresh)) + (pl.BlockSpec(memory_space=pltpu.VMEM),),
        input_output_aliases={i: n_sem + i for i in range(n_in)}, compiler_params=_IN_FLIGHT,
    )(*[pltpu.with_memory_space_constraint(o, pltpu.HBM) for o in rider.operands], *after)
    return (rider, res[:n_sem], res[n_sem:n_sem + n_in], res[n_sem + n_in:-1]), res[-1]


def _split_wait(handles, after, name):
    rider, sems, thru, fresh_arrays = handles
    n_in, n_out, n_sem = len(rider.operands), len(rider.out_shapes), len(rider.scratch)
    fresh = [j for j in range(n_out) if j not in rider.aliases.values()]
    by_out = {j: i for i, j in rider.aliases.items()}
    n_data = n_in + len(fresh)

    def body(*refs):
        r_in = refs[:n_in]
        fresh_refs = refs[n_in:n_data]
        sem_refs = refs[n_data:n_data + n_sem]
        r_out = [r_in[by_out[j]] if j in by_out else fresh_refs[fresh.index(j)] for j in range(n_out)]
        rider.wait(r_in, r_out, _as_rider_sems(rider, sem_refs))

    data = list(thru) + list(fresh_arrays)
    res = pl.pallas_call(
        body, name=name, out_shape=tuple(pltpu.HBM(d.shape, d.dtype) for d in data),
        in_specs=(HBM,) * n_data + (SEM,) * n_sem + (ANY,) * len(after), out_specs=(HBM,) * n_data,
        input_output_aliases={i: i for i in range(n_data)}, compiler_params=_IN_FLIGHT,
    )(*data, *sems, *after)
    return [res[by_out[j]] if j in by_out else res[n_in + fresh.index(j)] for j in range(n_out)], res[:n_in]


def _pair_gather_rider(bufs):
    n = len(bufs)

    def copies(r_out, sems):
        x, y, c = _mesh_pos()
        out = []
        for a in range(n):
            send = functools.partial(
                    pltpu.make_async_remote_copy,
                src_ref=r_out[a].at[c], dst_ref=r_out[a].at[c], send_sem=sems[0].at[a],
                recv_sem=sems[1].at[a], device_id=(x, y, 1 - c), device_id_type=MESH)
            recv = functools.partial(
                    pltpu.make_async_remote_copy,
                src_ref=r_out[a].at[1 - c], dst_ref=r_out[a].at[1 - c], send_sem=sems[0].at[a],
                recv_sem=sems[1].at[a], device_id=(x, y, 1 - c), device_id_type=MESH)
            out.append((send, recv))
        return out

    def start(r_in, r_out, sems):
        for send, _ in copies(r_out, sems):
            send().start()

    def wait(r_in, r_out, sems):
        cps = copies(r_out, sems)
        for _, recv in cps:
            recv().wait_recv()
        for send, _ in cps:
            send().wait_send()

    return _Rider(bufs, [jax.ShapeDtypeStruct(b.shape, b.dtype) for b in bufs], {i: i for i in range(n)},
                  [pltpu.SemaphoreType.DMA((n,)), pltpu.SemaphoreType.DMA((n,))], start, wait)


def _add_own_half(g, recv, pos_arr, name):
    nb, rh, cols = g.shape[0], g.shape[-2], g.shape[-1]

    def body(pos_ref, g_ref, r_ref, send_ref, own_ref):
        s = (g_ref[...] + r_ref[...]).astype(BF16)
        send_ref[...] = s

        @pl.when(pl.program_id(0) == pos_ref[1])
        def _():
            own_ref[...] = s

    blk = pl.BlockSpec((None, rh, cols), lambda j, pos_ref: (j, 0, 0))
    g_spec = blk if g.ndim == 3 else pl.BlockSpec((None, None, rh, cols),
                                                   lambda j, pos_ref: (j, pos_ref[0], 0, 0))
    shape = jax.ShapeDtypeStruct((nb, rh, cols), BF16)
    return pl.pallas_call(
        body, name=name,
        grid_spec=pltpu.PrefetchScalarGridSpec(
            num_scalar_prefetch=1, grid=(nb,), in_specs=[g_spec, blk],
            out_specs=[blk, pl.BlockSpec((None, rh, cols), lambda j, pos_ref: (pos_ref[1], 0, 0))]),
        out_shape=[shape, shape], compiler_params=_params(),
    )(pos_arr, g, recv)


def _sum_chips(gath, pos_arr, name):
    nb, rh, cols = gath.shape

    def body(pos_ref, a_ref, b_ref, c_ref, d_ref, o_ref):
        del pos_ref
        o_ref[...] = ((a_ref[...].astype(F32) + b_ref[...].astype(F32)) + c_ref[...].astype(F32)) \
            + d_ref[...].astype(F32)

    tr = rh // 2 if (rh // 2) % 16 == 0 else rh
    specs = [pl.BlockSpec((None, tr, cols), functools.partial(lambda i, pos_ref, j: (j, i, 0), j=j))
             for j in range(nb)]
    return pl.pallas_call(
        body, name=name,
        grid_spec=pltpu.PrefetchScalarGridSpec(
            num_scalar_prefetch=1, grid=(rh // tr,), in_specs=specs,
            out_specs=pl.BlockSpec((None, tr, cols), lambda i, pos_ref: (pos_ref[0], i, 0))),
        out_shape=jax.ShapeDtypeStruct((2, rh, cols), F32), compiler_params=_params(),
    )(pos_arr, gath, gath, gath, gath)


def _small_allreduce(v, name, rider=None, after=()):
    n = v.shape[0]
    n_dev = 8

    def body(v_ref, o_ref, buf, send_sems, recv_sems):
        x, y, c = _mesh_pos()
        me = 4 * x + 2 * y + c
        buf[me] = v_ref[...]
        sends = []
        peers = []
        for r in range(1, n_dev):
            px = 1 - x if r & 4 else x
            py = 1 - y if r & 2 else y
            pc = 1 - c if r & 1 else c
            peers.append((px, py, pc))
            cp = pltpu.make_async_remote_copy(
                src_ref=v_ref, dst_ref=buf.at[me], send_sem=send_sems.at[r - 1],
                recv_sem=recv_sems.at[r - 1], device_id=(px, py, pc), device_id_type=MESH)
            cp.start()
            sends.append(cp)
        for r, (px, py, pc) in enumerate(peers):
            pltpu.make_async_remote_copy(
                src_ref=v_ref, dst_ref=buf.at[4 * px + 2 * py + pc], send_sem=send_sems.at[r],
                recv_sem=recv_sems.at[r], device_id=(px, py, pc), device_id_type=MESH).wait_recv()
        for cp in sends:
            cp.wait_send()
        acc = buf[0]
        for i in range(1, n_dev):
            acc = acc + buf[i]
        o_ref[...] = acc

    whole = pl.BlockSpec(v.shape, lambda i: (0, 0))
    return _pallas(
        body, name=name, grid=(1,), in_specs=[whole], out_specs=whole,
        out_shape=jax.ShapeDtypeStruct(v.shape, v.dtype), operands=[v],
        scratch_shapes=[pltpu.VMEM((n_dev, n, LANES), F32), pltpu.SemaphoreType.DMA((n_dev - 1,)),
                        pltpu.SemaphoreType.DMA((n_dev - 1,))],
        rider=rider, after=after)


def _adamw_math(w, g, m, v):
    m = ADAM_B1 * m + (1.0 - ADAM_B1) * g
    v = ADAM_B2 * v + (1.0 - ADAM_B2) * (g * g)
    m_hat = m / (1.0 - ADAM_B1 ** ADAM_STEP)
    v_hat = v / (1.0 - ADAM_B2 ** ADAM_STEP)
    delta = -ADAM_LR * (m_hat / (jnp.sqrt(v_hat) + ADAM_EPS) + ADAM_WD * w)
    return delta, m, v


def _adamw(w, g, m, v, name, after=()):
    r, c = w.shape
    tr = 128 if r % 128 == 0 else 64
    assert r % tr == 0

    def body(w_ref, g_ref, m_ref, v_ref, go_ref, d_ref, mo_ref, vo_ref):
        gv = g_ref[...]
        d, mn, vn = _adamw_math(w_ref[...], gv, m_ref[...], v_ref[...])
        go_ref[...] = gv
        d_ref[...] = d
        mo_ref[...] = mn
        vo_ref[...] = vn

    blk = pl.BlockSpec((tr, c), lambda i: (i, 0))
    return _pallas(body, name=name, grid=(r // tr,), in_specs=[blk] * 4, out_specs=[blk] * 4,
                   out_shape=[jax.ShapeDtypeStruct((r, c), F32)] * 4, operands=[w, g, m, v], after=after)


def _adamw_small(ws, gs, ms, vs, name):
    n = len(ws)

    def body(*refs):
        w_r, g_r, m_r, v_r = refs[:n], refs[n:2 * n], refs[2 * n:3 * n], refs[3 * n:4 * n]
        d_o, m_o, v_o = refs[4 * n:5 * n], refs[5 * n:6 * n], refs[6 * n:7 * n]
        for i in range(n):
            d, mn, vn = _adamw_math(w_r[i][...], g_r[i][...], m_r[i][...], v_r[i][...])
            d_o[i][...] = d
            m_o[i][...] = mn
            v_o[i][...] = vn

    specs = [pl.BlockSpec(w.shape, lambda i: (0, 0)) for w in ws]
    shapes = [jax.ShapeDtypeStruct(w.shape, F32) for w in ws]
    outs = pl.pallas_call(
        body, name=name, grid=(1,), in_specs=specs * 4, out_specs=specs * 3, out_shape=shapes * 3,
        compiler_params=_params(),
    )(*ws, *gs, *ms, *vs)
    return outs[:n], outs[n:2 * n], outs[2 * n:]


BIG = ("w_in", "w_o_attn", "w_pw_conv", "w_out", "w_ffn_in", "w_ffn_out")
ROW_SHARDED = ("w_out", "w_ffn_out")
SMALL = ("norm1_w", "b_gate", "q_norm_w", "k_norm_w", "conv_w", "conv_b", "conv_ln_w", "conv_ln_b", "norm2_w")
ORDER = ("norm1_w", "w_in", "b_gate", "q_norm_w", "k_norm_w", "w_o_attn", "conv_w", "conv_b", "conv_ln_w",
         "conv_ln_b", "w_pw_conv", "w_out", "norm2_w", "w_ffn_in", "w_ffn_out")
PACK_TILE = 8 * LANES


def _pack_small(parts):
    rows = []
    for p in parts:
        flat = p.reshape(-1)
        pad = (-flat.shape[0]) % PACK_TILE
        rows.append(jnp.pad(flat, (0, pad)).reshape(-1, LANES))
    return jnp.concatenate(rows, axis=0)


def _unpack_small(packed, shapes):
    out, row = [], 0
    for shp in shapes:
        size = int(np.prod(shp))
        nrow = -(-size // PACK_TILE) * (PACK_TILE // LANES)
        out.append(packed[row:row + nrow].reshape(-1)[:size].reshape(shp))
        row += nrow
    return out


def kernel(x, positions, norm1_w, w_in, b_gate, q_norm_w, k_norm_w, w_o_attn, conv_w, conv_b, conv_ln_w, conv_ln_b, w_pw_conv, w_out, norm2_w, w_ffn_in, w_ffn_out, loss_target, m_norm1_w, m_w_in, m_b_gate, m_q_norm_w, m_k_norm_w, m_w_o_attn, m_conv_w, m_conv_b, m_conv_ln_w, m_conv_ln_b, m_w_pw_conv, m_w_out, m_norm2_w, m_w_ffn_in, m_w_ffn_out, v_norm1_w, v_w_in, v_b_gate, v_q_norm_w, v_k_norm_w, v_w_o_attn, v_conv_w, v_conv_b, v_conv_ln_w, v_conv_ln_b, v_w_pw_conv, v_w_out, v_norm2_w, v_w_ffn_in, v_w_ffn_out):
    w = dict(norm1_w=norm1_w, w_in=w_in, b_gate=b_gate, q_norm_w=q_norm_w, k_norm_w=k_norm_w, w_o_attn=w_o_attn,
             conv_w=conv_w, conv_b=conv_b, conv_ln_w=conv_ln_w, conv_ln_b=conv_ln_b, w_pw_conv=w_pw_conv,
             w_out=w_out, norm2_w=norm2_w, w_ffn_in=w_ffn_in, w_ffn_out=w_ffn_out)
    m = dict(norm1_w=m_norm1_w, w_in=m_w_in, b_gate=m_b_gate, q_norm_w=m_q_norm_w, k_norm_w=m_k_norm_w,
             w_o_attn=m_w_o_attn, conv_w=m_conv_w, conv_b=m_conv_b, conv_ln_w=m_conv_ln_w,
             conv_ln_b=m_conv_ln_b, w_pw_conv=m_w_pw_conv, w_out=m_w_out, norm2_w=m_norm2_w,
             w_ffn_in=m_w_ffn_in, w_ffn_out=m_w_ffn_out)
    v = dict(norm1_w=v_norm1_w, w_in=v_w_in, b_gate=v_b_gate, q_norm_w=v_q_norm_w, k_norm_w=v_k_norm_w,
             w_o_attn=v_w_o_attn, conv_w=v_conv_w, conv_b=v_conv_b, conv_ln_w=v_conv_ln_w,
             conv_ln_b=v_conv_ln_b, w_pw_conv=v_w_pw_conv, w_out=v_w_out, norm2_w=v_norm2_w,
             w_ffn_in=v_w_ffn_in, w_ffn_out=v_w_ffn_out)
    cx, cy, cc = _mesh_pos()
    chip = 2 * cx + cy

    chip_arr = chip.reshape(1).astype(jnp.int32)
    pos_arr = jnp.stack([cc, chip]).astype(jnp.int32)
    bufs = {}
    for n in BIG:
        buf = _cast_into_slot(w[n][0], chip_arr, BF16, f"cast_{n}")
        bufs[n] = buf.reshape(N_CHIPS, 2, buf.shape[1] // 2, buf.shape[2])
    small_bufs = [_cast_into_slot(w[n][0], chip_arr, F32, f"slot_{n}") for n in ("conv_w", "b_gate")]
    w_in_buf, conv_w_buf, b_gate_buf = _comm_call(_gather_both_legs_rider([bufs["w_in"]], small_bufs),
                                                  "allgather_w_in")
    wts = dict(w_in=w_in_buf.reshape(N_CHIPS, -1, w_in_buf.shape[3]),
               conv_w=conv_w_buf.transpose(1, 0, 2).reshape(CONV_WIDTH, -1),
               b_gate=b_gate_buf.transpose(1, 0, 2).reshape(2, 1, -1),
               norm1_w=norm1_w, q_norm_w=q_norm_w, k_norm_w=k_norm_w, conv_b=conv_b, conv_ln_w=conv_ln_w,
               conv_ln_b=conv_ln_b, norm2_w=norm2_w)

    loss, grad_x, g, reduced, w_in_in_flight = _forward_backward(
        x[0], positions.reshape(-1, 1), loss_target[0], wts, [bufs[n] for n in LATE_GATHER], pos_arr)
    grads = {n: b.reshape(-1, b.shape[2]) for n, b in reduced.items()}

    w_in_in_flight, started = w_in_in_flight
    delta, new_m, new_v = {}, {}, {}
    for n in EARLY_REDUCE:
        grads[n], delta[n], new_m[n], new_v[n] = _adamw(w[n][0], grads[n], m[n][0], v[n][0], f"adamw_{n}",
                                                        after=[started])
    small_parts = [loss] + [g[n] for n in SMALL]
    small_shapes = [p.shape for p in small_parts]
    summed = _small_allreduce(_pack_small(small_parts), "small_allreduce",
                              after=[delta[n] for n in EARLY_REDUCE])
    reduced = _unpack_small(summed, small_shapes)
    loss_total = reduced[0].reshape(())
    for n, r in zip(SMALL, reduced[1:]):
        grads[n] = r
    ch_shard = conv_w.shape[2]
    grads["conv_w"] = lax.dynamic_slice_in_dim(grads["conv_w"], chip * ch_shard, ch_shard, axis=1)
    d_shard = b_gate.shape[2]
    grads["b_gate"] = lax.dynamic_slice_in_dim(grads["b_gate"], chip * d_shard, d_shard, axis=1)

    (by_chip_w_in,), _ = _split_wait(w_in_in_flight, after=[delta[n] for n in EARLY_REDUCE] + [summed],
                                     name="grads_w_in_exchange_wait")
    half_w_in = _sum_chips(by_chip_w_in, pos_arr, "grads_chip_sum_w_in")
    (shard_w_in,) = _comm_call(_pair_gather_rider([half_w_in]), "grads_pair_gather_w_in")
    grads["w_in"], delta["w_in"], new_m["w_in"], new_v["w_in"] = _adamw(
        w["w_in"][0], shard_w_in.reshape(-1, shard_w_in.shape[2]), m["w_in"][0], v["w_in"][0], "adamw_w_in")
    flat2 = lambda a: a.reshape(-1, a.shape[-1])
    d_s, m_s, v_s = _adamw_small([flat2(w[n]) for n in SMALL], [flat2(grads[n]) for n in SMALL],
                                 [flat2(m[n]) for n in SMALL], [flat2(v[n]) for n in SMALL], "adamw_small")
    for i, n in enumerate(SMALL):
        delta[n], new_m[n], new_v[n] = d_s[i], m_s[i], v_s[i]

    shaped = lambda d, n: d[n].reshape(w[n].shape)
    return (loss_total, grad_x[None], *[shaped(grads, n) for n in ORDER], *[shaped(delta, n) for n in ORDER],
            *[shaped(new_m, n) for n in ORDER], *[shaped(new_v, n) for n in ORDER])
```

```python
import functools

import numpy as np
import jax
import jax.numpy as jnp
from jax import lax
from jax.experimental import pallas as pl
from jax.experimental.pallas import tpu as pltpu

F32 = jnp.float32
BF16 = jnp.bfloat16
MESH = pl.DeviceIdType.MESH
ANY = pl.BlockSpec(memory_space=pl.ANY)

HEAD_DIM = 64
N_SLOT_HEADS = 8
DILATIONS = (1, 4, 16)
HALF_SPAN = 64
ROPE_THETA = 500000.0
ROT_DIM = 16
CONV_WIDTH = 31
EPS = 1e-6
NEG_INF = -1e30
ADAM_LR, ADAM_B1, ADAM_B2, ADAM_EPS, ADAM_WD, ADAM_STEP = 0.001, 0.9, 0.999, 1e-08, 0.01, 10

LANES = 128
QBLK = 128
KWIN = QBLK + 2 * HALF_SPAN
VMEM_LIMIT = 48 * 1024 * 1024
N_CHIPS = 4


def _params(**kw):
    return pltpu.CompilerParams(vmem_limit_bytes=VMEM_LIMIT, **kw)


class _Rider:
    def __init__(self, operands, out_shapes, aliases, scratch, start, wait):
        self.operands, self.out_shapes, self.aliases = list(operands), list(out_shapes), dict(aliases)
        self.scratch, self.start, self.wait = list(scratch), start, wait


def _pallas(body, *, name, grid, in_specs, out_specs, out_shape, operands, scratch_shapes=(), aliases=None,
            rider=None, after=()):
    single = not isinstance(out_specs, (list, tuple))
    out_specs_l = [out_specs] if single else list(out_specs)
    out_shape_l = [out_shape] if single else list(out_shape)
    aliases = dict(aliases or {})

    def call(fn, all_in_specs, all_out_specs, all_out_shape, all_scratch, all_aliases, all_operands):
        return pl.pallas_call(
            fn, name=name, grid=grid, in_specs=all_in_specs, out_specs=all_out_specs, out_shape=all_out_shape,
            scratch_shapes=all_scratch, input_output_aliases=all_aliases, compiler_params=_params(),
        )(*all_operands)

    if rider is None:
        n_main = len(in_specs)

        def ordered(*refs):
            body(*refs[:n_main], *refs[n_main + len(after):])

        res = call(ordered if after else body, list(in_specs) + [ANY] * len(after), out_specs_l, out_shape_l,
                   list(scratch_shapes), aliases, list(operands) + list(after))
        return res[0] if single else res
    assert not after
    n_in, n_rin = len(in_specs), len(rider.operands)
    n_out, n_rout = len(out_specs_l), len(rider.out_shapes)
    n_sc = len(scratch_shapes)

    def wrapped(*refs):
        main_in, r_in = refs[:n_in], refs[n_in:n_in + n_rin]
        o0 = n_in + n_rin
        main_out, r_out = refs[o0:o0 + n_out], refs[o0 + n_out:o0 + n_out + n_rout]
        s0 = o0 + n_out + n_rout
        main_sc, r_sc = refs[s0:s0 + n_sc], refs[s0 + n_sc:]
        ids = [pl.program_id(d) for d in range(len(grid))]
        first = functools.reduce(jnp.logical_and, [i == 0 for i in ids])
        last = functools.reduce(jnp.logical_and, [i == n - 1 for i, n in zip(ids, grid)])

        @pl.when(first)
        def _():
            rider.start(r_in, r_out, r_sc)

        body(*main_in, *main_out, *main_sc)

        @pl.when(last)
        def _():
            rider.wait(r_in, r_out, r_sc)

    for src, dst in rider.aliases.items():
        aliases[n_in + src] = n_out + dst
    res = call(wrapped, list(in_specs) + [ANY] * n_rin, out_specs_l + [ANY] * n_rout,
               out_shape_l + rider.out_shapes, list(scratch_shapes) + rider.scratch, aliases,
               list(operands) + rider.operands)
    main = res[:n_out]
    return (main[0] if single else main), res[n_out:]


def _matmul(a, b, *, mode, tm, tn, tk, out_dtype, name, b_blocked=False,
            out_blocked=None, cols_outer=False, rider=None, after=()):
    a_shape = a.shape
    if mode == "nn":
        m_dim, k_dim = a_shape
        n_dim = b.shape[0] * b.shape[2] if b_blocked else b.shape[1]
        rows, cols, red = m_dim, n_dim, k_dim
    elif mode == "nt":
        m_dim, n_dim = a_shape
        k_dim = b.shape[1] if b_blocked else b.shape[0]
        rows, cols, red = m_dim, k_dim, n_dim
    else:
        m_dim, k_dim = a_shape
        n_dim = b.shape[1]
        rows, cols, red = k_dim, n_dim, m_dim
    assert rows % tm == 0 and cols % tn == 0 and red % tk == 0, (name, rows, cols, red)
    ni, nj, nk = rows // tm, cols // tn, red // tk

    if mode == "nn":
        a_spec = pl.BlockSpec((tm, tk), lambda i, j, k: (i, k))
        if b_blocked:
            per = b.shape[2] // tn
            b_spec = pl.BlockSpec((None, tk, tn), lambda i, j, k: (j // per, k, j % per))
        else:
            b_spec = pl.BlockSpec((tk, tn), lambda i, j, k: (k, j))
        dims = (((1,), (0,)), ((), ()))
    elif mode == "nt":
        a_spec = pl.BlockSpec((tm, tk), lambda i, j, k: (i, k))
        if b_blocked:
            per = b.shape[2] // tk
            b_spec = pl.BlockSpec((None, tn, tk), lambda i, j, k: (k // per, j, k % per))
        else:
            b_spec = pl.BlockSpec((tn, tk), lambda i, j, k: (j, k))
        dims = (((1,), (1,)), ((), ()))
    else:
        a_spec = pl.BlockSpec((tk, tm), lambda i, j, k: (k, i))
        b_spec = pl.BlockSpec((tk, tn), lambda i, j, k: (k, j))
        dims = (((0,), (0,)), ((), ()))

    if out_blocked:
        per_o = (cols // out_blocked) // tn
        out_spec = pl.BlockSpec((None, tm, tn), lambda i, j, k: (j // per_o, i, j % per_o))
        out_shape = jax.ShapeDtypeStruct((out_blocked, rows, cols // out_blocked), out_dtype)
    else:
        out_spec = pl.BlockSpec((tm, tn), lambda i, j, k: (i, j))
        out_shape = jax.ShapeDtypeStruct((rows, cols), out_dtype)

    def body(a_ref, b_ref, o_ref, *acc):
        prod = lax.dot_general(a_ref[...], b_ref[...], dims, preferred_element_type=F32)
        if nk == 1:
            o_ref[...] = prod.astype(out_dtype)
        else:
            acc_ref, = acc
            k = pl.program_id(2)

            @pl.when(k == 0)
            def _():
                acc_ref[...] = prod

            @pl.when(k > 0)
            def _():
                acc_ref[...] += prod

            @pl.when(k == nk - 1)
            def _():
                o_ref[...] = acc_ref[...].astype(out_dtype)

    scratch = [pltpu.VMEM((tm, tn), F32)] if nk > 1 else []
    grid = (ni, nj, nk)
    if cols_outer:
        swap = lambda spec: pl.BlockSpec(spec.block_shape, lambda j, i, k, f=spec.index_map: f(i, j, k))
        a_spec, b_spec, out_spec, grid = swap(a_spec), swap(b_spec), swap(out_spec), (nj, ni, nk)
    return _pallas(body, name=name, grid=grid, in_specs=[a_spec, b_spec], out_specs=out_spec,
                   out_shape=out_shape, operands=[a, b], scratch_shapes=scratch, rider=rider, after=after)


def _rmsnorm_fwd(x, w, name):
    s, d = x.shape
    tm = 256

    def body(x_ref, w_ref, o_ref):
        xv = x_ref[...]
        rstd = lax.rsqrt(jnp.mean(xv * xv, axis=-1, keepdims=True) + EPS)
        o_ref[...] = (xv * rstd * w_ref[...]).astype(BF16)

    return pl.pallas_call(
        body, name=name, grid=(s // tm,),
        in_specs=[pl.BlockSpec((tm, d), lambda i: (i, 0)), pl.BlockSpec((1, d), lambda i: (0, 0))],
        out_specs=pl.BlockSpec((tm, d), lambda i: (i, 0)),
        out_shape=jax.ShapeDtypeStruct((s, d), BF16), compiler_params=_params(),
    )(x, w)


def _rmsnorm_bwd(dh, x, w, dres, name, rider=None):
    s, d = x.shape
    tm = 256

    def body(dh_ref, x_ref, w_ref, dres_ref, dx_ref, dxb_ref, dw_ref):
        xv = x_ref[...]
        rstd = lax.rsqrt(jnp.mean(xv * xv, axis=-1, keepdims=True) + EPS)
        xhat = xv * rstd
        dhv = dh_ref[...]
        g = dhv * w_ref[...]
        dx = rstd * (g - xhat * jnp.mean(g * xhat, axis=-1, keepdims=True)) + dres_ref[...]
        dx_ref[...] = dx
        dxb_ref[...] = dx.astype(BF16)
        part = jnp.sum(dhv * xhat, axis=0, keepdims=True)

        @pl.when(pl.program_id(0) == 0)
        def _():
            dw_ref[...] = part

        @pl.when(pl.program_id(0) > 0)
        def _():
            dw_ref[...] += part

    row = pl.BlockSpec((tm, d), lambda i: (i, 0))
    vec = pl.BlockSpec((1, d), lambda i: (0, 0))
    return _pallas(
        body, name=name, grid=(s // tm,), in_specs=[row, row, vec, row], out_specs=[row, row, vec],
        out_shape=[jax.ShapeDtypeStruct((s, d), F32), jax.ShapeDtypeStruct((s, d), BF16),
                   jax.ShapeDtypeStruct((1, d), F32)],
        operands=[dh, x, w, dres], rider=rider)


def _rope_consts():
    lane = np.arange(LANES)
    in_head = lane % HEAD_DIM
    inv_freq = ROPE_THETA ** (-jnp.arange(0, ROT_DIM, 2, dtype=F32) / ROT_DIM)
    invf = jnp.where(jnp.asarray(in_head < ROT_DIM), jnp.tile(inv_freq, LANES // (ROT_DIM // 2)), 0.0)
    m_a = np.where(in_head < ROT_DIM // 2, -1.0, 0.0).astype(np.float32)
    m_b = np.where((in_head >= ROT_DIM // 2) & (in_head < ROT_DIM), 1.0, 0.0).astype(np.float32)
    block_diag = (lane[:, None] // HEAD_DIM == lane[None, :] // HEAD_DIM).astype(np.float32)
    return (invf.reshape(1, LANES).astype(F32), jnp.asarray(m_a).reshape(1, LANES),
            jnp.asarray(m_b).reshape(1, LANES), jnp.asarray(block_diag, dtype=BF16))


def _head_sums(v, bd):
    hi = v.astype(BF16)
    lo = (v - hi.astype(F32)).astype(BF16)
    return jnp.dot(hi, bd, preferred_element_type=F32) + jnp.dot(lo, bd, preferred_element_type=F32)


def _qk_fwd(proj, pos_col, qw2, kw2, consts, name, rider=None):
    s = proj.shape[0]
    width = 3 * N_SLOT_HEADS * HEAD_DIM
    tm = 128
    invf, m_a, m_b, bd = consts
    scale = HEAD_DIM ** -0.5

    def body(q_ref, k_ref, pos_ref, qw_ref, kw_ref, invf_ref, ma_ref, mb_ref, bd_ref, qo_ref, ko_ref):
        ang = pos_ref[...].astype(F32) * invf_ref[...]
        cos = jnp.cos(ang)
        sin = jnp.sin(ang)
        s_a = sin * ma_ref[...]
        s_b = sin * mb_ref[...]
        bdv = bd_ref[...]
        for src, w_ref, dst, sc in ((q_ref, qw_ref, qo_ref, scale), (k_ref, kw_ref, ko_ref, 1.0)):
            for cb in range(width // LANES):
                cols = slice(cb * LANES, (cb + 1) * LANES)
                t = src[:, cols]
                rstd = lax.rsqrt(_head_sums(t * t, bdv) * (1.0 / HEAD_DIM) + EPS)
                y = t * rstd * w_ref[...]
                r = y * cos + pltpu.roll(y, LANES - 8, axis=1) * s_a + pltpu.roll(y, 8, axis=1) * s_b
                dst[:, cols] = r * sc if sc != 1.0 else r

    vec = pl.BlockSpec((1, LANES), lambda i: (0, 0))
    return _pallas(
        body, name=name, grid=(s // tm,),
        in_specs=[pl.BlockSpec((tm, width), lambda i: (i, 0)), pl.BlockSpec((tm, width), lambda i: (i, 1)),
                  pl.BlockSpec((tm, 1), lambda i: (i, 0)), vec, vec, vec, vec, vec,
                  pl.BlockSpec((LANES, LANES), lambda i: (0, 0))],
        out_specs=[pl.BlockSpec((tm, width), lambda i: (i, 0))] * 2,
        out_shape=[jax.ShapeDtypeStruct((s, width), F32)] * 2,
        operands=[proj, proj, pos_col, qw2, kw2, invf, m_a, m_b, bd], rider=rider)


def _qk_bwd(dqn, dkn, dv, da, db, dgl, proj, pos_col, qw2, kw2, consts, name, rider=None):
    s = proj.shape[0]
    width = 3 * N_SLOT_HEADS * HEAD_DIM
    ch = da.shape[1]
    gate_w = dgl.shape[1]
    out_w = 3 * width + 2 * ch + gate_w
    assert out_w == proj.shape[1]
    tm = 128
    invf, m_a, m_b, bd = consts
    scale = HEAD_DIM ** -0.5

    def body(dq_ref, dk_ref, dv_ref, da_ref, db_ref, dgl_ref, q_ref, k_ref, pos_ref, qw_ref, kw_ref,
             invf_ref, ma_ref, mb_ref, bd_ref, out_ref, dqw_ref, dkw_ref):
        ang = pos_ref[...].astype(F32) * invf_ref[...]
        cos = jnp.cos(ang)
        sin = jnp.sin(ang)
        s_a = sin * ma_ref[...]
        s_b = sin * mb_ref[...]
        bdv = bd_ref[...]
        first = pl.program_id(0) == 0
        for src, dsrc, w_ref, col0, dw_ref, sc in ((q_ref, dq_ref, qw_ref, 0, dqw_ref, scale),
                                                   (k_ref, dk_ref, kw_ref, width, dkw_ref, 1.0)):
            dw_acc = jnp.zeros((1, LANES), F32)
            for cb in range(width // LANES):
                cols = slice(cb * LANES, (cb + 1) * LANES)
                t = src[:, cols]
                dr = dsrc[:, cols]
                if sc != 1.0:
                    dr = dr * sc
                dy = dr * cos + pltpu.roll(dr * s_a, 8, axis=1) + pltpu.roll(dr * s_b, LANES - 8, axis=1)
                rstd = lax.rsqrt(_head_sums(t * t, bdv) * (1.0 / HEAD_DIM) + EPS)
                xhat = t * rstd
                g = dy * w_ref[...]
                dt = rstd * (g - xhat * (_head_sums(g * xhat, bdv) * (1.0 / HEAD_DIM)))
                out_ref[:, col0 + cb * LANES: col0 + (cb + 1) * LANES] = dt.astype(BF16)
                dw_acc = dw_acc + jnp.sum(dy * xhat, axis=0, keepdims=True)
            dw_acc = dw_acc + pltpu.roll(dw_acc, HEAD_DIM, axis=1)

            @pl.when(first)
            def _(dw_ref=dw_ref, dw_acc=dw_acc):
                dw_ref[...] = dw_acc

            @pl.when(jnp.logical_not(first))
            def _(dw_ref=dw_ref, dw_acc=dw_acc):
                dw_ref[...] += dw_acc
        out_ref[:, 2 * width: 3 * width] = dv_ref[...].astype(BF16)
        out_ref[:, 3 * width: 3 * width + ch] = da_ref[...]
        out_ref[:, 3 * width + ch: 3 * width + 2 * ch] = db_ref[...]
        out_ref[:, 3 * width + 2 * ch: out_w] = dgl_ref[...]

    vec = pl.BlockSpec((1, LANES), lambda i: (0, 0))
    blk = lambda c: pl.BlockSpec((tm, width), lambda i: (i, c))
    cblk = pl.BlockSpec((tm, ch), lambda i: (i, 0))
    return _pallas(
        body, name=name, grid=(s // tm,),
        in_specs=[blk(0), blk(0), blk(0), cblk, cblk, pl.BlockSpec((tm, gate_w), lambda i: (i, 0)),
                  blk(0), blk(1), pl.BlockSpec((tm, 1), lambda i: (i, 0)), vec, vec, vec, vec, vec,
                  pl.BlockSpec((LANES, LANES), lambda i: (0, 0))],
        out_specs=[pl.BlockSpec((tm, out_w), lambda i: (i, 0)), vec, vec],
        out_shape=[jax.ShapeDtypeStruct((s, out_w), BF16)] + [jax.ShapeDtypeStruct((1, LANES), F32)] * 2,
        operands=[dqn, dkn, dv, da, db, dgl, proj, proj, pos_col, qw2, kw2, invf, m_a, m_b, bd],
        rider=rider)


def _row_chunks(n_rows, fn, chunk=256):
    def step(i, c):
        fn(pl.ds(pl.multiple_of(i * chunk, chunk), chunk))
        return c
    lax.fori_loop(0, n_rows // chunk, step, 0)


def _to_residue_major(dst, src, s, d, dst_off=0, cast=None):
    seq = s // d
    for r in range(d):
        v = src[...] if d == 1 else src[pl.ds(r, seq, stride=d), :]
        dst[dst_off + r * seq: dst_off + (r + 1) * seq, :] = v if cast is None else v.astype(cast)


def _from_residue_major(dst, src, s, d, src_off=0):
    seq = s // d
    for r in range(d):
        v = src[src_off + r * seq: src_off + (r + 1) * seq, :]
        if d == 1:
            dst[...] = v
        else:
            dst[pl.ds(r, seq, stride=d), :] = v


def _band_bias():
    qi = lax.broadcasted_iota(jnp.int32, (QBLK, KWIN), 0)
    kj = lax.broadcasted_iota(jnp.int32, (QBLK, KWIN), 1)
    return jnp.where(jnp.abs(kj - HALF_SPAN - qi) <= HALF_SPAN, 0.0, NEG_INF).astype(F32)


def _range_bias(base, seq):
    kj = lax.broadcasted_iota(jnp.int32, (1, KWIN), 1)
    lo = (base & -seq) - base + HALF_SPAN
    return jnp.where((kj >= lo) & (kj < lo + seq), 0.0, NEG_INF).astype(F32)


ATTN_BLOCKS_PER_TRIP = 8


def _skewed_blocks(n_blk, produce, consume):
    per = ATTN_BLOCKS_PER_TRIP
    produce(0, 0)

    def trip(i, carry):
        for u in range(per):
            consume(per * i + u, u % 2)
            produce(per * i + u + 1, (u + 1) % 2)
        return carry

    n_trips = n_blk // per - 1
    lax.fori_loop(0, n_trips, trip, 0)
    for b in range(per * n_trips, n_blk):
        consume(b, b % 2)
        if b + 1 < n_blk:
            produce(b + 1, (b + 1) % 2)


def _block_base(b):
    return b * QBLK if isinstance(b, int) else pl.multiple_of(b * QBLK, QBLK)


def _attn_fwd(qn, kn, proj, name, rider=None):
    s = qn.shape[0]
    n_pairs = N_SLOT_HEADS * HEAD_DIM // LANES
    v_col0 = 2 * qn.shape[1] // LANES
    nt_dims = (((1,), (1,)), ((), ()))

    def body(q_ref, k_ref, v_ref, attn_ref, lse_ref, attn_b_ref, q_rm, k_rm, v_rm, acc_rm, m_rm, l_rm,
             acc_p, m_p, l_p, m_run, l_run, acc_run, band, s_buf, m_buf):
        g = pl.program_id(1)
        zpad = jnp.zeros((HALF_SPAN, LANES), BF16)
        k_rm[0:HALF_SPAN, :] = zpad
        k_rm[s + HALF_SPAN: s + 2 * HALF_SPAN, :] = zpad
        v_rm[0:HALF_SPAN, 0:LANES] = zpad
        v_rm[s + HALF_SPAN: s + 2 * HALF_SPAN, 0:LANES] = zpad

        def ones_rows(rows):
            v_rm[pl.ds(rows.start, rows.size), LANES:2 * LANES] = jnp.ones((rows.size, LANES), BF16)

        _row_chunks(s + 2 * HALF_SPAN, ones_rows, chunk=2 * HALF_SPAN)
        band[...] = _band_bias()
        lane = lax.broadcasted_iota(jnp.int32, (QBLK, LANES), 1)
        low = lane < HEAD_DIM
        n_blk = s // QBLK

        for gi, d in enumerate(DILATIONS):
            @pl.when(g == gi)
            def _(gi=gi, d=d):
                seq = s // d
                _to_residue_major(q_rm, q_ref, s, d, cast=BF16)
                _to_residue_major(k_rm, k_ref, s, d, dst_off=HALF_SPAN, cast=BF16)
                _to_residue_major(v_rm.at[:, 0:LANES], v_ref, s, d, dst_off=HALF_SPAN, cast=BF16)

                def scores(b, slot):
                    base = _block_base(b)
                    q = q_rm[pl.ds(base, QBLK), :]
                    zero = jnp.zeros_like(q)
                    q2 = jnp.concatenate([jnp.where(low, q, zero), jnp.where(low, zero, q)], axis=0)
                    sc = lax.dot_general(q2, k_rm[pl.ds(base, KWIN), :], nt_dims, preferred_element_type=F32)
                    bias = band[...] + _range_bias(base, seq)
                    for hh in range(2):
                        rows = slice(hh * QBLK, (hh + 1) * QBLK)
                        sh = sc[rows, :] + bias
                        s_buf[slot, rows, :] = sh
                        m_buf[slot, rows, :] = jnp.broadcast_to(jnp.max(sh, axis=-1, keepdims=True), (QBLK, LANES))

                def outputs(b, slot):
                    base = _block_base(b)
                    sv = s_buf[slot]
                    mb = m_buf[slot]
                    p = jnp.exp(jnp.concatenate([sv[:, 0:LANES] - mb, sv[:, LANES:2 * LANES] - mb], axis=1))
                    pv = jnp.dot(p.astype(BF16), v_rm[pl.ds(base, KWIN), :], preferred_element_type=F32)
                    rows = pl.ds(base, QBLK)
                    acc_rm[rows, :] = jnp.where(low, pv[0:QBLK, 0:LANES], pv[QBLK:2 * QBLK, 0:LANES])
                    l_rm[rows, :] = jnp.where(low, pv[0:QBLK, LANES:2 * LANES], pv[QBLK:2 * QBLK, LANES:2 * LANES])
                    m_rm[rows, :] = jnp.where(low, mb[0:QBLK, :], mb[QBLK:2 * QBLK, :])

                _skewed_blocks(n_blk, scores, outputs)
                if d == 1:
                    src = (acc_rm, m_rm, l_rm)
                else:
                    for dst_, src_ in ((acc_p, acc_rm), (m_p, m_rm), (l_p, l_rm)):
                        _from_residue_major(dst_, src_, s, d)
                    src = (acc_p, m_p, l_p)

                def combine(rows):
                    a_g, m_g, l_g = src[0][rows, :], src[1][rows, :], src[2][rows, :]
                    if gi == 0:
                        m_new, l_new, a_new = m_g, l_g, a_g
                    else:
                        m_old = m_run[rows, :]
                        m_new = jnp.maximum(m_old, m_g)
                        w_old = jnp.exp(m_old - m_new)
                        w_g = jnp.exp(m_g - m_new)
                        l_new = l_run[rows, :] * w_old + l_g * w_g
                        a_new = acc_run[rows, :] * w_old + a_g * w_g
                    if gi == len(DILATIONS) - 1:
                        out = a_new / l_new
                        attn_ref[rows, :] = out
                        attn_b_ref[rows, :] = out.astype(BF16)
                        lse_ref[rows, :] = m_new + jnp.log(l_new)
                    else:
                        m_run[rows, :] = m_new
                        l_run[rows, :] = l_new
                        acc_run[rows, :] = a_new

                _row_chunks(s, combine)

    qk_spec = pl.BlockSpec((s, LANES), lambda hp, g: (0, g * n_pairs + hp))
    v_spec = pl.BlockSpec((s, LANES), lambda hp, g: (0, v_col0 + g * n_pairs + hp))
    o_spec = pl.BlockSpec((s, LANES), lambda hp, g: (0, hp))
    f32buf = pltpu.VMEM((s, LANES), F32)
    return _pallas(
        body, name=name, grid=(n_pairs, len(DILATIONS)), in_specs=[qk_spec, qk_spec, v_spec],
        out_specs=[o_spec, o_spec, o_spec],
        out_shape=[jax.ShapeDtypeStruct((s, n_pairs * LANES), F32)] * 2
        + [jax.ShapeDtypeStruct((s, n_pairs * LANES), BF16)],
        operands=[qn, kn, proj],
        scratch_shapes=[pltpu.VMEM((s, LANES), BF16), pltpu.VMEM((s + 2 * HALF_SPAN, LANES), BF16),
                        pltpu.VMEM((s + 2 * HALF_SPAN, 2 * LANES), BF16)] + [f32buf] * 9
        + [pltpu.VMEM((QBLK, KWIN), F32), pltpu.VMEM((2, 2 * QBLK, KWIN), F32),
           pltpu.VMEM((2, 2 * QBLK, LANES), F32)],
        rider=rider)


def _attn_bwd(qn, kn, proj, dattn, attn, lse, bd, name, rider=None):
    s = qn.shape[0]
    n_pairs = N_SLOT_HEADS * HEAD_DIM // LANES
    v_col0 = 2 * qn.shape[1] // LANES
    nt_dims = (((1,), (1,)), ((), ()))
    tn_dims = (((0,), (0,)), ((), ()))
    spad = s + 2 * HALF_SPAN

    def body(q_ref, k_ref, v_ref, do_ref, o_ref, lse_ref, bd_ref, dq_ref, dk_ref, dv_ref,
             q_rm, k_rm, v_rm, do_rm, lse0_rm, lse1_rm, dd0_rm, dd1_rm, dq_rm, dk_rm, dv_rm,
             lse0_p, lse1_p, dd0_p, dd1_p, band, p_buf, ds_buf):
        g = pl.program_id(1)
        zpad = jnp.zeros((HALF_SPAN, LANES), BF16)
        for buf in (k_rm, v_rm):
            buf[0:HALF_SPAN, :] = zpad
            buf[s + HALF_SPAN: spad, :] = zpad
        zf = jnp.zeros((HALF_SPAN, LANES), F32)
        for buf in (dk_rm, dv_rm):
            buf[0:HALF_SPAN, :] = zf
            buf[s + HALF_SPAN: spad, :] = zf
        band[...] = _band_bias()

        def clear(rows):
            z = jnp.zeros((rows.size, LANES), F32)
            dk_rm[pl.ds(rows.start + HALF_SPAN, rows.size), :] = z
            dv_rm[pl.ds(rows.start + HALF_SPAN, rows.size), :] = z

        _row_chunks(s, clear)

        def prepare(rows):
            lo = lax.broadcasted_iota(jnp.int32, (rows.size, LANES), 1) < HEAD_DIM
            dsum = _head_sums(do_ref[rows, :] * o_ref[rows, :], bd_ref[...])
            dswap = pltpu.roll(dsum, HEAD_DIM, axis=1)
            dd0_p[rows, :] = jnp.where(lo, dsum, dswap)
            dd1_p[rows, :] = jnp.where(lo, dswap, dsum)
            lv = lse_ref[rows, :]
            lswap = pltpu.roll(lv, HEAD_DIM, axis=1)
            lse0_p[rows, :] = jnp.where(lo, lv, lswap)
            lse1_p[rows, :] = jnp.where(lo, lswap, lv)

        @pl.when(g == 0)
        def _():
            _row_chunks(s, prepare)
        lane = lax.broadcasted_iota(jnp.int32, (QBLK, LANES), 1)
        low = lane < HEAD_DIM
        n_blk = s // QBLK

        def stacked(ref, rows):
            val = ref[rows, :]
            zero = jnp.zeros_like(val)
            return jnp.concatenate([jnp.where(low, val, zero), jnp.where(low, zero, val)], axis=0)

        for gi, d in enumerate(DILATIONS):
            @pl.when(g == gi)
            def _(d=d):
                seq = s // d
                _to_residue_major(q_rm, q_ref, s, d, cast=BF16)
                _to_residue_major(k_rm, k_ref, s, d, dst_off=HALF_SPAN, cast=BF16)
                _to_residue_major(v_rm, v_ref, s, d, dst_off=HALF_SPAN, cast=BF16)
                _to_residue_major(do_rm, do_ref, s, d, cast=BF16)
                for dst_, src_ in ((lse0_rm, lse0_p), (lse1_rm, lse1_p), (dd0_rm, dd0_p), (dd1_rm, dd1_p)):
                    _to_residue_major(dst_, src_, s, d)

                def scores(b, slot):
                    base = _block_base(b)
                    rows = pl.ds(base, QBLK)
                    win = pl.ds(base, KWIN)
                    sc = lax.dot_general(stacked(q_rm, rows), k_rm[win, :], nt_dims, preferred_element_type=F32)
                    dp = lax.dot_general(stacked(do_rm, rows), v_rm[win, :], nt_dims, preferred_element_type=F32)
                    bias = band[...] + _range_bias(base, seq)
                    for hh, (lse_r, dd_r) in enumerate(((lse0_rm, dd0_rm), (lse1_rm, dd1_rm))):
                        r = slice(hh * QBLK, (hh + 1) * QBLK)
                        lse_h = lse_r[rows, :]
                        dd_h = dd_r[rows, :]
                        sh = sc[r, :] + bias
                        p = jnp.exp(jnp.concatenate([sh[:, 0:LANES] - lse_h, sh[:, LANES:KWIN] - lse_h], axis=1))
                        dph = dp[r, :]
                        ds = p * jnp.concatenate([dph[:, 0:LANES] - dd_h, dph[:, LANES:KWIN] - dd_h], axis=1)
                        p_buf[slot, r, :] = p.astype(BF16)
                        ds_buf[slot, r, :] = ds.astype(BF16)

                def grads(b, slot):
                    base = _block_base(b)
                    rows = pl.ds(base, QBLK)
                    win = pl.ds(base, KWIN)
                    p = p_buf[slot]
                    ds = ds_buf[slot]
                    dq2 = jnp.dot(ds, k_rm[win, :], preferred_element_type=F32)
                    dq_rm[rows, :] = jnp.where(low, dq2[0:QBLK, :], dq2[QBLK:2 * QBLK, :])
                    dk_rm[win, :] += lax.dot_general(ds, stacked(q_rm, rows), tn_dims, preferred_element_type=F32)
                    dv_rm[win, :] += lax.dot_general(p, stacked(do_rm, rows), tn_dims, preferred_element_type=F32)

                _skewed_blocks(n_blk, scores, grads)
                _from_residue_major(dq_ref, dq_rm, s, d)
                _from_residue_major(dk_ref, dk_rm, s, d, src_off=HALF_SPAN)
                _from_residue_major(dv_ref, dv_rm, s, d, src_off=HALF_SPAN)

    qk_spec = pl.BlockSpec((s, LANES), lambda hp, g: (0, g * n_pairs + hp))
    v_spec = pl.BlockSpec((s, LANES), lambda hp, g: (0, v_col0 + g * n_pairs + hp))
    o_spec = pl.BlockSpec((s, LANES), lambda hp, g: (0, hp))
    width = qn.shape[1]
    f32buf = pltpu.VMEM((s, LANES), F32)
    f32pad = pltpu.VMEM((spad, LANES), F32)
    return _pallas(
        body, name=name, grid=(n_pairs, len(DILATIONS)),
        in_specs=[qk_spec, qk_spec, v_spec, o_spec, o_spec, o_spec,
                  pl.BlockSpec((LANES, LANES), lambda hp, g: (0, 0))],
        out_specs=[qk_spec, qk_spec, qk_spec],
        out_shape=[jax.ShapeDtypeStruct((s, width), F32)] * 3,
        operands=[qn, kn, proj, dattn, attn, lse, bd],
        scratch_shapes=[pltpu.VMEM((s, LANES), BF16), pltpu.VMEM((spad, LANES), BF16),
                        pltpu.VMEM((spad, LANES), BF16), pltpu.VMEM((s, LANES), BF16),
                        f32buf, f32buf, f32buf, f32buf, f32buf, f32pad, f32pad,
                        f32buf, f32buf, f32buf, f32buf, pltpu.VMEM((QBLK, KWIN), F32),
                        pltpu.VMEM((2, 2 * QBLK, KWIN), BF16), pltpu.VMEM((2, 2 * QBLK, KWIN), BF16)],
        rider=rider)


CONV_PAD = 16


def _conv_fwd(proj, conv_w, conv_b, col0, name, rider=None):
    s = proj.shape[0]
    ch = conv_w.shape[1]
    nblk = ch // LANES
    a0 = col0 // LANES
    tr = 256
    shift = CONV_PAD - (CONV_WIDTH - 1) // 2

    def body(a_ref, b_ref, w_ref, bias_ref, u0_ref, uc_ref, pad):
        z = jnp.zeros((CONV_PAD, LANES), F32)
        pad[0:CONV_PAD, :] = z
        pad[s + CONV_PAD: s + 2 * CONV_PAD, :] = z

        def glu(rows):
            u0 = a_ref[rows, :] * jax.nn.sigmoid(b_ref[rows, :])
            u0_ref[rows, :] = u0
            pad[pl.ds(rows.start + CONV_PAD, rows.size), :] = u0

        _row_chunks(s, glu)
        for t in range(0, s, tr):
            acc = jnp.broadcast_to(bias_ref[...], (tr, LANES))
            for k in range(CONV_WIDTH):
                acc = acc + w_ref[k:k + 1, :] * pad[t + k + shift: t + k + shift + tr, :]
            uc_ref[t:t + tr, :] = acc

    return _pallas(
        body, name=name, grid=(nblk,),
        in_specs=[pl.BlockSpec((s, LANES), lambda c: (0, a0 + c)),
                  pl.BlockSpec((s, LANES), lambda c: (0, a0 + nblk + c)),
                  pl.BlockSpec((CONV_WIDTH, LANES), lambda c: (0, c)),
                  pl.BlockSpec((1, LANES), lambda c: (0, c))],
        out_specs=[pl.BlockSpec((s, LANES), lambda c: (0, c))] * 2,
        out_shape=[jax.ShapeDtypeStruct((s, ch), F32)] * 2, operands=[proj, proj, conv_w, conv_b],
        scratch_shapes=[pltpu.VMEM((s + 2 * CONV_PAD, LANES), F32)], rider=rider)


def _ln_silu_fwd(uc, ln_w, ln_b, name):
    s, ch = uc.shape
    tm = 256

    def body(u_ref, w_ref, b_ref, o_ref):
        u = u_ref[...]
        mu = jnp.mean(u, axis=-1, keepdims=True)
        xc = u - mu
        rstd = lax.rsqrt(jnp.mean(xc * xc, axis=-1, keepdims=True) + EPS)
        z = xc * rstd * w_ref[...] + b_ref[...]
        o_ref[...] = (z * jax.nn.sigmoid(z)).astype(BF16)

    row = pl.BlockSpec((tm, ch), lambda i: (i, 0))
    vec = pl.BlockSpec((1, ch), lambda i: (0, 0))
    return pl.pallas_call(
        body, name=name, grid=(s // tm,), in_specs=[row, vec, vec], out_specs=row,
        out_shape=jax.ShapeDtypeStruct((s, ch), BF16), compiler_params=_params(),
    )(uc, ln_w, ln_b)


def _ln_silu_bwd(du3, uc, ln_w, ln_b, name):
    s, ch = uc.shape
    tm = 256

    def body(d_ref, u_ref, w_ref, b_ref, du_ref, dw_ref, db_ref):
        u = u_ref[...]
        mu = jnp.mean(u, axis=-1, keepdims=True)
        xc = u - mu
        rstd = lax.rsqrt(jnp.mean(xc * xc, axis=-1, keepdims=True) + EPS)
        xhat = xc * rstd
        z = xhat * w_ref[...] + b_ref[...]
        sg = jax.nn.sigmoid(z)
        dz = d_ref[...] * (sg * (1.0 + z * (1.0 - sg)))
        dxh = dz * w_ref[...]
        du_ref[...] = rstd * (dxh - jnp.mean(dxh, axis=-1, keepdims=True)
                              - xhat * jnp.mean(dxh * xhat, axis=-1, keepdims=True))
        pw = jnp.sum(dz * xhat, axis=0, keepdims=True)
        pb = jnp.sum(dz, axis=0, keepdims=True)
        first = pl.program_id(0) == 0

        @pl.when(first)
        def _():
            dw_ref[...] = pw
            db_ref[...] = pb

        @pl.when(jnp.logical_not(first))
        def _():
            dw_ref[...] += pw
            db_ref[...] += pb

    row = pl.BlockSpec((tm, ch), lambda i: (i, 0))
    vec = pl.BlockSpec((1, ch), lambda i: (0, 0))
    return pl.pallas_call(
        body, name=name, grid=(s // tm,), in_specs=[row, row, vec, vec], out_specs=[row, vec, vec],
        out_shape=[jax.ShapeDtypeStruct((s, ch), F32), jax.ShapeDtypeStruct((1, ch), F32),
                   jax.ShapeDtypeStruct((1, ch), F32)],
        compiler_params=_params(),
    )(du3, uc, ln_w, ln_b)


def _conv_bwd(duc, u0, proj, conv_w, col0, name, rider=None):
    s = proj.shape[0]
    ch = conv_w.shape[1]
    nblk = ch // LANES
    a0 = col0 // LANES
    tr = 256
    half = (CONV_WIDTH - 1) // 2
    shift = CONV_PAD - half

    def body(duc_ref, u0_ref, a_ref, b_ref, w_ref, da_ref, db_ref, dw_ref, dbias_ref, pad_d, pad_u):
        z = jnp.zeros((CONV_PAD, LANES), F32)
        for buf in (pad_d, pad_u):
            buf[0:CONV_PAD, :] = z
            buf[s + CONV_PAD: s + 2 * CONV_PAD, :] = z

        def fill(rows):
            dst = pl.ds(rows.start + CONV_PAD, rows.size)
            pad_d[dst, :] = duc_ref[rows, :]
            pad_u[dst, :] = u0_ref[rows, :]

        _row_chunks(s, fill)
        dw_acc = [jnp.zeros((8, LANES), F32) for _ in range(CONV_WIDTH)]
        dbias_acc = jnp.zeros((8, LANES), F32)
        for t in range(0, s, tr):
            d_t = duc_ref[t:t + tr, :]
            dbias_acc = dbias_acc + jnp.sum(d_t.reshape(tr // 8, 8, LANES), axis=0)
            du0 = jnp.zeros((tr, LANES), F32)
            for k in range(CONV_WIDTH):
                du0 = du0 + w_ref[k:k + 1, :] * pad_d[t - k + half + CONV_PAD: t - k + half + CONV_PAD + tr, :]
                prod = d_t * pad_u[t + k + shift: t + k + shift + tr, :]
                dw_acc[k] = dw_acc[k] + jnp.sum(prod.reshape(tr // 8, 8, LANES), axis=0)
            av = a_ref[t:t + tr, :]
            sg = jax.nn.sigmoid(b_ref[t:t + tr, :])
            da_ref[t:t + tr, :] = (du0 * sg).astype(BF16)
            db_ref[t:t + tr, :] = (du0 * av * sg * (1.0 - sg)).astype(BF16)
        for k in range(CONV_WIDTH):
            dw_ref[k:k + 1, :] = jnp.sum(dw_acc[k], axis=0, keepdims=True)
        dbias_ref[...] = jnp.sum(dbias_acc, axis=0, keepdims=True)

    col = lambda off: pl.BlockSpec((s, LANES), lambda c: (0, off + c))
    return _pallas(
        body, name=name, grid=(nblk,),
        in_specs=[col(0), col(0), col(a0), col(a0 + nblk),
                  pl.BlockSpec((CONV_WIDTH, LANES), lambda c: (0, c))],
        out_specs=[col(0), col(0), pl.BlockSpec((CONV_WIDTH, LANES), lambda c: (0, c)),
                   pl.BlockSpec((1, LANES), lambda c: (0, c))],
        out_shape=[jax.ShapeDtypeStruct((s, ch), BF16)] * 2
        + [jax.ShapeDtypeStruct((CONV_WIDTH, ch), F32), jax.ShapeDtypeStruct((1, ch), F32)],
        operands=[duc, u0, proj, proj, conv_w],
        scratch_shapes=[pltpu.VMEM((s + 2 * CONV_PAD, LANES), F32)] * 2, rider=rider)


GATE_BLK = 512


def _gate_fwd(proj, bg, y_a, y_b, col0, name):
    s, d = y_a.shape
    tm = 256
    g0 = col0 // GATE_BLK
    nb = d // GATE_BLK

    def body(ga_ref, gb_ref, ba_ref, bb_ref, ya_ref, yb_ref, o_ref):
        g_a = jax.nn.sigmoid(ga_ref[...] + ba_ref[...])
        g_b = jax.nn.sigmoid(gb_ref[...] + bb_ref[...])
        o_ref[...] = (g_a * ya_ref[...] + g_b * yb_ref[...]).astype(BF16)

    act = pl.BlockSpec((tm, GATE_BLK), lambda i, j: (i, j))
    return pl.pallas_call(
        body, name=name, grid=(s // tm, nb),
        in_specs=[pl.BlockSpec((tm, GATE_BLK), lambda i, j: (i, g0 + j)),
                  pl.BlockSpec((tm, GATE_BLK), lambda i, j: (i, g0 + nb + j)),
                  pl.BlockSpec((None, 1, GATE_BLK), lambda i, j: (0, 0, j)),
                  pl.BlockSpec((None, 1, GATE_BLK), lambda i, j: (1, 0, j)), act, act],
        out_specs=act, out_shape=jax.ShapeDtypeStruct((s, d), BF16), compiler_params=_params(),
    )(proj, proj, bg, bg, y_a, y_b)


def _gate_bwd(d_mixed, proj, bg, y_a, y_b, col0, name, rider=None):
    s, d = y_a.shape
    tm = 256
    half = d // 2
    assert col0 % half == 0
    c0 = col0 // half

    def body(dm_ref, a0_ref, a1_ref, b0_ref, b1_ref, bias_ref, ya_ref, yb_ref, dgl_ref, dya_ref, dyb_ref, db_ref):
        dm = dm_ref[...]
        parts = []
        for br, (lo_ref, hi_ref, y_ref, dy_ref) in enumerate(((a0_ref, a1_ref, ya_ref, dya_ref),
                                                              (b0_ref, b1_ref, yb_ref, dyb_ref))):
            logits = jnp.concatenate([lo_ref[...], hi_ref[...]], axis=1)
            gate = jax.nn.sigmoid(logits + bias_ref[br])
            dy_ref[...] = (dm * gate).astype(BF16)
            dgl = dm * y_ref[...] * gate * (1.0 - gate)
            dgl_ref[:, br * d:(br + 1) * d] = dgl.astype(BF16)
            parts.append(jnp.sum(dgl, axis=0, keepdims=True))
        part = jnp.concatenate(parts, axis=0)
        first = pl.program_id(0) == 0

        @pl.when(first)
        def _():
            db_ref[...] = part

        @pl.when(jnp.logical_not(first))
        def _():
            db_ref[...] += part

    row = pl.BlockSpec((tm, d), lambda i: (i, 0))
    logit_blk = lambda k: pl.BlockSpec((tm, half), functools.partial(lambda i, k: (i, c0 + k), k=k))
    return _pallas(
        body, name=name, grid=(s // tm,),
        in_specs=[row, logit_blk(0), logit_blk(1), logit_blk(2), logit_blk(3),
                  pl.BlockSpec((2, 1, d), lambda i: (0, 0, 0)), row, row],
        out_specs=[pl.BlockSpec((tm, 2 * d), lambda i: (i, 0)), row, row, pl.BlockSpec((2, d), lambda i: (0, 0))],
        out_shape=[jax.ShapeDtypeStruct((s, 2 * d), BF16), jax.ShapeDtypeStruct((s, d), BF16),
                   jax.ShapeDtypeStruct((s, d), BF16), jax.ShapeDtypeStruct((2, d), F32)],
        operands=[d_mixed, proj, proj, proj, proj, bg, y_a, y_b], rider=rider)


def _ffn_in_swiglu(h2, w_blocked, name):
    s, k = h2.shape
    nblk, _, tn = w_blocked.shape
    ff = nblk // 2 * tn
    tm = 512

    def body(a_ref, wg_ref, wu_ref, g_ref, u_ref, act_ref):
        a = a_ref[...]
        gt = jnp.dot(a, wg_ref[...], preferred_element_type=F32)
        up = jnp.dot(a, wu_ref[...], preferred_element_type=F32)
        g_ref[...] = gt
        u_ref[...] = up
        act_ref[...] = (gt * jax.nn.sigmoid(gt) * up).astype(BF16)

    out = pl.BlockSpec((tm, tn), lambda j, i: (i, j))
    return pl.pallas_call(
        body, name=name, grid=(nblk // 2, s // tm),
        in_specs=[pl.BlockSpec((tm, k), lambda j, i: (i, 0)),
                  pl.BlockSpec((None, k, tn), lambda j, i: (j, 0, 0)),
                  pl.BlockSpec((None, k, tn), lambda j, i: (nblk // 2 + j, 0, 0))],
        out_specs=[out, out, out],
        out_shape=[jax.ShapeDtypeStruct((s, ff), F32), jax.ShapeDtypeStruct((s, ff), F32),
                   jax.ShapeDtypeStruct((s, ff), BF16)],
        compiler_params=_params(),
    )(h2, w_blocked, w_blocked)


def _swiglu_bwd(gate, up, d_act, name, rider=None):
    s, ff = gate.shape
    tm = 256

    def body(g_ref, u_ref, d_ref, o_ref):
        gt = g_ref[...]
        sg = jax.nn.sigmoid(gt)
        dv = d_ref[...]
        o_ref[:, 0:ff] = (dv * u_ref[...] * (sg * (1.0 + gt * (1.0 - sg)))).astype(BF16)
        o_ref[:, ff:2 * ff] = (dv * gt * sg).astype(BF16)

    row = pl.BlockSpec((tm, ff), lambda i: (i, 0))
    return _pallas(
        body, name=name, grid=(s // tm,), in_specs=[row, row, row],
        out_specs=pl.BlockSpec((tm, 2 * ff), lambda i: (i, 0)),
        out_shape=jax.ShapeDtypeStruct((s, 2 * ff), BF16), operands=[gate, up, d_act], rider=rider)


def _out_proj_rmsnorm(mixed, w_out, x, norm_w, name):
    s, k = mixed.shape
    d = w_out.shape[1]
    tm = 512

    def body(a_ref, w_ref, x_ref, nw_ref, x1_ref, h2_ref):
        x1 = x_ref[...] + jnp.dot(a_ref[...], w_ref[...], preferred_element_type=F32)
        x1_ref[...] = x1
        rstd = lax.rsqrt(jnp.mean(x1 * x1, axis=-1, keepdims=True) + EPS)
        h2_ref[...] = (x1 * rstd * nw_ref[...]).astype(BF16)

    row = pl.BlockSpec((tm, d), lambda i: (i, 0))
    return pl.pallas_call(
        body, name=name, grid=(s // tm,),
        in_specs=[pl.BlockSpec((tm, k), lambda i: (i, 0)), pl.BlockSpec((k, d), lambda i: (0, 0)), row,
                  pl.BlockSpec((1, d), lambda i: (0, 0))],
        out_specs=[row, row],
        out_shape=[jax.ShapeDtypeStruct((s, d), F32), jax.ShapeDtypeStruct((s, d), BF16)],
        compiler_params=_params(),
    )(mixed, w_out, x, norm_w)


def _ffn_out_loss(act, w_ffn_out, x1, target, name):
    s, k = act.shape
    d = w_ffn_out.shape[1]
    tm = 512

    def body(a_ref, w_ref, x1_ref, t_ref, dy_ref, dyb_ref, loss_ref, acc):
        y = x1_ref[...] + jnp.dot(a_ref[...], w_ref[...], preferred_element_type=F32)
        diff = y - t_ref[...]
        dy = diff * (1.0 / d)
        dy_ref[...] = dy
        dyb_ref[...] = dy.astype(BF16)
        part = jnp.sum((diff * diff).reshape(tm // 8, 8, d), axis=0)
        i = pl.program_id(0)

        @pl.when(i == 0)
        def _():
            acc[...] = part

        @pl.when(i > 0)
        def _():
            acc[...] += part

        @pl.when(i == pl.num_programs(0) - 1)
        def _():
            loss_ref[...] = (0.5 / d) * jnp.sum(jnp.sum(acc[...], axis=1, keepdims=True), axis=0, keepdims=True)

    row = pl.BlockSpec((tm, d), lambda i: (i, 0))
    return pl.pallas_call(
        body, name=name, grid=(s // tm,),
        in_specs=[pl.BlockSpec((tm, k), lambda i: (i, 0)), pl.BlockSpec((k, d), lambda i: (0, 0)), row, row],
        out_specs=[row, row, pl.BlockSpec((1, 1), lambda i: (0, 0))],
        out_shape=[jax.ShapeDtypeStruct((s, d), F32), jax.ShapeDtypeStruct((s, d), BF16),
                   jax.ShapeDtypeStruct((1, 1), F32)],
        scratch_shapes=[pltpu.VMEM((8, d), F32)], compiler_params=_params(),
    )(act, w_ffn_out, x1, target)


LATE_GATHER = ("w_o_attn", "w_pw_conv", "w_out", "w_ffn_in", "w_ffn_out")
EARLY_REDUCE = LATE_GATHER


def _blocks_by_half(g):
    if g.ndim == 2:
        g = g.reshape(N_CHIPS, g.shape[0] // N_CHIPS, g.shape[1])
    return g.reshape(N_CHIPS, 2, g.shape[1] // 2, g.shape[2])


def _forward_backward(x, pos_col, target, wts, late_bufs, pos_arr):
    wts = dict(wts)
    consts = _rope_consts()
    bd = consts[3]
    qw2 = jnp.tile(wts["q_norm_w"], (1, LANES // HEAD_DIM))
    kw2 = jnp.tile(wts["k_norm_w"], (1, LANES // HEAD_DIM))
    qkv_w = 3 * N_SLOT_HEADS * HEAD_DIM
    conv_col0 = 3 * qkv_w
    ch = wts["conv_w"].shape[1]
    gate_col0 = conv_col0 + 2 * ch

    h = _rmsnorm_fwd(x, wts["norm1_w"], "rms1_fwd")
    gathering, started = _split_start(_gather_ici_rider(late_bufs, []), "late_gather_start", after=[wts["w_in"]])
    proj = _matmul(h, wts["w_in"], mode="nn", tm=512, tn=1920, tk=1024, out_dtype=F32, name="mm_proj",
                   b_blocked=True, cols_outer=True, after=[started])
    qn, kn = _qk_fwd(proj, pos_col, qw2, kw2, consts, "qk_fwd")
    attn, lse, attn_b = _attn_fwd(qn, kn, proj, "attn_fwd")
    late_bufs, _ = _split_wait(gathering, after=[attn_b], name="late_gather_wait")
    (u0, uc), late_bufs = _conv_fwd(proj, wts["conv_w"], wts["conv_b"], conv_col0, "conv_fwd",
                                    rider=_gather_forward_rider(late_bufs))
    for n, buf in zip(LATE_GATHER, late_bufs):
        full = buf.reshape(N_CHIPS, -1, buf.shape[3])
        wts[n] = full.reshape(-1, full.shape[2]) if n in ROW_SHARDED else full
    y_a = _matmul(attn_b, wts["w_o_attn"], mode="nn", tm=1024, tn=256, tk=512, out_dtype=F32, name="mm_ya",
                  b_blocked=True)
    u3 = _ln_silu_fwd(uc, wts["conv_ln_w"], wts["conv_ln_b"], "ln_fwd")
    y_b = _matmul(u3, wts["w_pw_conv"], mode="nn", tm=1024, tn=256, tk=512, out_dtype=F32, name="mm_yb",
                  b_blocked=True)
    mixed = _gate_fwd(proj, wts["b_gate"], y_a, y_b, gate_col0, "gate_fwd")
    x1, h2 = _out_proj_rmsnorm(mixed, wts["w_out"], x, wts["norm2_w"], "mm_x1_rms2")
    gate, up, act = _ffn_in_swiglu(h2, wts["w_ffn_in"], "mm_gu_swiglu")
    dy, dy_b16, loss = _ffn_out_loss(act, wts["w_ffn_out"], x1, target, "mm_x2_loss")

    g = {}
    by_chip = {}

    def pair_add(n, blocks, received):
        return _add_own_half(blocks, received, pos_arr, f"grads_pair_add_{n}")

    d_act = _matmul(dy_b16, wts["w_ffn_out"], mode="nt", tm=512, tn=1408, tk=1024, out_dtype=F32, name="mm_dact",
                    cols_outer=True)
    g_ffn_out = _blocks_by_half(
        _matmul(act, dy_b16, mode="tn", tm=1408, tn=1024, tk=2048, out_dtype=F32, name="mm_dwffnout"))
    dgu, (received,) = _swiglu_bwd(gate, up, d_act, "swiglu_bwd",
                                   rider=_pair_exchange_rider([g_ffn_out], halved=True))
    to_send, own = pair_add("w_ffn_out", g_ffn_out, received)
    dh2, (by_chip["w_ffn_out"],) = _matmul(
        dgu, wts["w_ffn_in"], mode="nt", tm=1024, tn=1024, tk=1408, out_dtype=F32, name="mm_dh2", b_blocked=True,
        rider=_chip_exchange_rider([to_send], [own]))
    g_ffn_in = _blocks_by_half(_matmul(h2, dgu, mode="tn", tm=512, tn=1408, tk=2048, out_dtype=F32,
                                       name="mm_dwffnin", out_blocked=N_CHIPS, cols_outer=True))
    exchanging, started = _split_start(_pair_exchange_rider([g_ffn_in], halved=True), "grads_ffn_in_pair_start")
    dx1, dx1_b16, g["norm2_w"] = _rmsnorm_bwd(dh2, x1, wts["norm2_w"], dy, "rms2_bwd")
    d_mixed = _matmul(dx1_b16, wts["w_out"], mode="nt", tm=512, tn=1024, tk=1024, out_dtype=F32, name="mm_dmixed",
                      after=[started])
    g["w_out"] = _matmul(mixed, dx1_b16, mode="tn", tm=512, tn=1024, tk=2048, out_dtype=F32, name="mm_dwout")
    dgl, dy_a, dy_b, g["b_gate"] = _gate_bwd(d_mixed, proj, wts["b_gate"], y_a, y_b, gate_col0, "gate_bwd")
    (received,), (g_ffn_in,) = _split_wait(exchanging, after=[dgl], name="grads_ffn_in_pair_wait")
    ffn_in_to_send, ffn_in_own = pair_add("w_ffn_in", g_ffn_in, received)
    dattn = _matmul(dy_a, wts["w_o_attn"], mode="nt", tm=1024, tn=512, tk=256, out_dtype=F32, name="mm_dattn",
                    b_blocked=True)
    g["w_o_attn"] = _matmul(attn_b, dy_a, mode="tn", tm=512, tn=256, tk=2048, out_dtype=F32, name="mm_dwo",
                            out_blocked=N_CHIPS)
    du3 = _matmul(dy_b, wts["w_pw_conv"], mode="nt", tm=1024, tn=512, tk=256, out_dtype=F32, name="mm_du3",
                  b_blocked=True)
    g["w_pw_conv"] = _matmul(u3, dy_b, mode="tn", tm=512, tn=256, tk=2048, out_dtype=F32, name="mm_dwpw",
                             out_blocked=N_CHIPS)
    duc, g["conv_ln_w"], g["conv_ln_b"] = _ln_silu_bwd(du3, uc, wts["conv_ln_w"], wts["conv_ln_b"], "ln_bwd")

    small3 = ("w_out", "w_o_attn", "w_pw_conv")
    g_small3 = [_blocks_by_half(g.pop(n)) for n in small3]
    (da, db, g["conv_w"], g["conv_b"]), received = _conv_bwd(
        duc, u0, proj, wts["conv_w"], conv_col0, "conv_bwd", rider=_pair_exchange_rider(g_small3, halved=True))
    sums3 = [pair_add(n, gb, rv) for n, gb, rv in zip(small3, g_small3, received)]
    (dqn, dkn, dv), (by_chip["w_ffn_in"],) = _attn_bwd(
        qn, kn, proj, dattn, attn, lse, bd, "attn_bwd",
        rider=_chip_exchange_rider([ffn_in_to_send], [ffn_in_own]))
    (dproj, dqw, dkw), exchanged3 = _qk_bwd(
        dqn, dkn, dv, da, db, dgl, proj, pos_col, qw2, kw2, consts, "qk_bwd",
        rider=_chip_exchange_rider([s[0] for s in sums3], [s[1] for s in sums3]))
    by_chip.update(zip(small3, exchanged3))
    halves = [_sum_chips(by_chip[n], pos_arr, f"grads_chip_sum_{n}") for n in EARLY_REDUCE]
    g["q_norm_w"] = dqw[:, :HEAD_DIM]
    g["k_norm_w"] = dkw[:, :HEAD_DIM]

    c = pos_arr[0]
    rh = h.shape[1] // 2
    h_sibling = lax.dynamic_slice_in_dim(h, (1 - c) * rh, rh, axis=1)
    h_own = lax.dynamic_slice_in_dim(h, c * rh, rh, axis=1)
    g_sibling, shards = _matmul(h_sibling, dproj, mode="tn", tm=rh, tn=1920, tk=2048, out_dtype=F32,
                                name="mm_dwin_sibling", out_blocked=N_CHIPS, rider=_pair_gather_rider(halves))
    reduced = dict(zip(EARLY_REDUCE, shards))
    exchanging, started = _split_start(_pair_exchange_rider([g_sibling], halved=False), "grads_w_in_pair_start")
    g_own = _matmul(h_own, dproj, mode="tn", tm=rh, tn=1920, tk=2048, out_dtype=F32, name="mm_dwin_own",
                    out_blocked=N_CHIPS, after=[started])
    (from_sibling,), _ = _split_wait(exchanging, after=[g_own], name="grads_w_in_pair_wait")
    to_send, own = _add_own_half(g_own, from_sibling, pos_arr, "grads_pair_add_w_in")
    in_flight, started = _split_start(_chip_exchange_rider([to_send], [own]), "grads_w_in_exchange_start")
    dh = _matmul(dproj, wts["w_in"], mode="nt", tm=1024, tn=1024, tk=1920, out_dtype=F32, name="mm_dh",
                 b_blocked=True, after=[started])
    grad_x, _, g["norm1_w"] = _rmsnorm_bwd(dh, x, wts["norm1_w"], dx1, "rms1_bwd")
    return loss, grad_x, g, reduced, (in_flight, started)


def _mesh_pos():
    return lax.axis_index("x"), lax.axis_index("y"), lax.axis_index("c")


def _other_chips(x, y):
    return [(1 - x, y), (x, 1 - y), (1 - x, 1 - y)]


def _cast_into_slot(shard, chip_arr, dtype, name):
    r, c = shard.shape
    tr = r // 2 if r % 32 == 0 else r

    def body(chip_ref, s_ref, o_ref):
        del chip_ref
        o_ref[...] = s_ref[...].astype(dtype)

    return pl.pallas_call(
        body, name=name,
        grid_spec=pltpu.PrefetchScalarGridSpec(
            num_scalar_prefetch=1, grid=(r // tr,),
            in_specs=[pl.BlockSpec((tr, c), lambda i, chip_ref: (i, 0))],
            out_specs=pl.BlockSpec((None, tr, c), lambda i, chip_ref: (chip_ref[0], i, 0))),
        out_shape=jax.ShapeDtypeStruct((N_CHIPS, r, c), dtype), compiler_params=_params(),
    )(chip_arr, shard)


def _gather_both_legs_rider(big, small):
    nb = len(big)
    n = nb + len(small)

    def part(bufs, a, slot, half):
        return bufs[a].at[slot, half] if a < nb else bufs[a].at[slot]

    def start(r_in, bufs, sems):
        x, y, c = _mesh_pos()
        me = 2 * x + y
        chips = _other_chips(x, y)
        for a in range(n):
            for k in range(3):
                px, py = chips[k]
                pltpu.make_async_remote_copy(
                    src_ref=part(bufs, a, me, c), dst_ref=part(bufs, a, me, c), send_sem=sems[0].at[a, k],
                    recv_sem=sems[1].at[a, k], device_id=(px, py, c), device_id_type=MESH).start()

    def wait(r_in, bufs, sems):
        x, y, c = _mesh_pos()
        me = 2 * x + y
        chips = _other_chips(x, y)
        sibling = (x, y, 1 - c)
        forwards = []
        for a in range(n):
            for k in range(3):
                px, py = chips[k]
                slot = 2 * px + py
                pltpu.make_async_remote_copy(
                    src_ref=part(bufs, a, slot, c), dst_ref=part(bufs, a, slot, c), send_sem=sems[0].at[a, k],
                    recv_sem=sems[1].at[a, k], device_id=(px, py, c), device_id_type=MESH).wait_recv()
                if a < nb:
                    fwd = pltpu.make_async_remote_copy(
                        src_ref=part(bufs, a, slot, c), dst_ref=part(bufs, a, slot, c), send_sem=sems[2].at[a, k],
                        recv_sem=sems[3].at[a, k], device_id=sibling, device_id_type=MESH)
                    fwd.start()
                    forwards.append(fwd)
        for a in range(nb):
            for k in range(3):
                px, py = chips[k]
                slot = 2 * px + py
                pltpu.make_async_remote_copy(
                    src_ref=part(bufs, a, slot, 1 - c), dst_ref=part(bufs, a, slot, 1 - c),
                    send_sem=sems[2].at[a, k], recv_sem=sems[3].at[a, k], device_id=sibling,
                    device_id_type=MESH).wait_recv()
        for a in range(n):
            for k in range(3):
                px, py = chips[k]
                pltpu.make_async_remote_copy(
                    src_ref=part(bufs, a, me, c), dst_ref=part(bufs, a, me, c), send_sem=sems[0].at[a, k],
                    recv_sem=sems[1].at[a, k], device_id=(px, py, c), device_id_type=MESH).wait_send()
        for fwd in forwards:
            fwd.wait_send()

    ops = list(big) + list(small)
    return _Rider(ops, [jax.ShapeDtypeStruct(o.shape, o.dtype) for o in ops], {i: i for i in range(n)},
                  [pltpu.SemaphoreType.DMA((n, 3)), pltpu.SemaphoreType.DMA((n, 3)),
                   pltpu.SemaphoreType.DMA((nb, 3)), pltpu.SemaphoreType.DMA((nb, 3))], start, wait)


def _comm_call(rider, name):
    def body():
        pass

    return _pallas(body, name=name, grid=(1,), in_specs=[], out_specs=[], out_shape=[], operands=[],
                   rider=rider)[1]


def _gather_ici_rider(big, small):
    nb = len(big)
    n = nb + len(small)

    def copies(bufs, sems):
        x, y, c = _mesh_pos()
        me = 2 * x + y
        part = lambda a, slot: bufs[a].at[slot, c] if a < nb else bufs[a].at[slot]
        out = []
        for a in range(n):
            for k, (px, py) in enumerate(_other_chips(x, y)):
                send = functools.partial(
                    pltpu.make_async_remote_copy,
                    src_ref=part(a, me), dst_ref=part(a, me), send_sem=sems[0].at[a, k],
                    recv_sem=sems[1].at[a, k], device_id=(px, py, c), device_id_type=MESH)
                recv = functools.partial(
                    pltpu.make_async_remote_copy,
                    src_ref=part(a, 2 * px + py), dst_ref=part(a, 2 * px + py), send_sem=sems[0].at[a, k],
                    recv_sem=sems[1].at[a, k], device_id=(px, py, c), device_id_type=MESH)
                out.append((send, recv))
        return out

    def start(r_in, r_out, sems):
        for send, _ in copies(r_out, sems):
            send().start()

    def wait(r_in, r_out, sems):
        cps = copies(r_out, sems)
        for _, recv in cps:
            recv().wait_recv()
        for send, _ in cps:
            send().wait_send()

    ops = list(big) + list(small)
    return _Rider(ops, [jax.ShapeDtypeStruct(o.shape, o.dtype) for o in ops], {i: i for i in range(n)},
                  [pltpu.SemaphoreType.DMA((n, 3)), pltpu.SemaphoreType.DMA((n, 3))], start, wait)


def _gather_forward_rider(big):
    n = len(big)

    def copies(bufs, sems):
        x, y, c = _mesh_pos()
        out = []
        for a in range(n):
            for k, (px, py) in enumerate(_other_chips(x, y)):
                slot = 2 * px + py
                send = functools.partial(
                    pltpu.make_async_remote_copy,
                    src_ref=bufs[a].at[slot, c], dst_ref=bufs[a].at[slot, c], send_sem=sems[0].at[a, k],
                    recv_sem=sems[1].at[a, k], device_id=(x, y, 1 - c), device_id_type=MESH)
                recv = functools.partial(
                    pltpu.make_async_remote_copy,
                    src_ref=bufs[a].at[slot, 1 - c], dst_ref=bufs[a].at[slot, 1 - c], send_sem=sems[0].at[a, k],
                    recv_sem=sems[1].at[a, k], device_id=(x, y, 1 - c), device_id_type=MESH)
                out.append((send, recv))
        return out

    def start(r_in, r_out, sems):
        for send, _ in copies(r_out, sems):
            send().start()

    def wait(r_in, r_out, sems):
        cps = copies(r_out, sems)
        for _, recv in cps:
            recv().wait_recv()
        for send, _ in cps:
            send().wait_send()

    return _Rider(big, [jax.ShapeDtypeStruct(o.shape, o.dtype) for o in big], {i: i for i in range(n)},
                  [pltpu.SemaphoreType.DMA((n, 3)), pltpu.SemaphoreType.DMA((n, 3))], start, wait)


def _pair_exchange_rider(gs, halved):
    n = len(gs)

    def copies(r_in, r_out, sems):
        x, y, c = _mesh_pos()
        return [pltpu.make_async_remote_copy(
            src_ref=r_in[a].at[:, 1 - c] if halved else r_in[a], dst_ref=r_out[a], send_sem=sems[0].at[a],
            recv_sem=sems[1].at[a], device_id=(x, y, 1 - c), device_id_type=MESH) for a in range(n)]

    def start(r_in, r_out, sems):
        for cp in copies(r_in, r_out, sems):
            cp.start()

    def wait(r_in, r_out, sems):
        for cp in copies(r_in, r_out, sems):
            cp.wait()

    return _Rider(gs, [jax.ShapeDtypeStruct((g.shape[0],) + g.shape[-2:], g.dtype) for g in gs], {},
                  [pltpu.SemaphoreType.DMA((n,)), pltpu.SemaphoreType.DMA((n,))], start, wait)


def _chip_exchange_rider(to_send, by_chip, row_range=None):
    n = len(to_send)

    def copies(r_in, r_out, sems):
        x, y, c = _mesh_pos()
        me = 2 * x + y
        rows = (lambda ref: ref) if row_range is None else (lambda ref: ref.at[pl.ds(*row_range)])
        out = []
        for a in range(n):
            for k, (px, py) in enumerate(_other_chips(x, y)):
                send = functools.partial(
                    pltpu.make_async_remote_copy,
                    src_ref=rows(r_in[a].at[2 * px + py]), dst_ref=rows(r_out[a].at[me]),
                    send_sem=sems[0].at[a, k], recv_sem=sems[1].at[a, k], device_id=(px, py, c),
                    device_id_type=MESH)
                recv = functools.partial(
                    pltpu.make_async_remote_copy,
                    src_ref=rows(r_in[a].at[me]), dst_ref=rows(r_out[a].at[2 * px + py]),
                    send_sem=sems[0].at[a, k], recv_sem=sems[1].at[a, k], device_id=(px, py, c),
                    device_id_type=MESH)
                out.append((send, recv))
        return out

    def start(r_in, r_out, sems):
        for send, _ in copies(r_in, r_out, sems):
            send().start()

    def wait(r_in, r_out, sems):
        cps = copies(r_in, r_out, sems)
        for _, recv in cps:
            recv().wait_recv()
        for send, _ in cps:
            send().wait_send()

    return _Rider(list(to_send) + list(by_chip), [jax.ShapeDtypeStruct(b.shape, b.dtype) for b in by_chip],
                  {n + i: i for i in range(n)},
                  [pltpu.SemaphoreType.DMA((n, 3)), pltpu.SemaphoreType.DMA((n, 3))], start, wait)


HBM = pl.BlockSpec(memory_space=pltpu.HBM)
SEM = pl.BlockSpec(memory_space=pltpu.SEMAPHORE)


_IN_FLIGHT = pltpu.CompilerParams(has_side_effects=pltpu.SideEffectType.DATAFLOW_SIDE_EFFECTING)


class _FlatSems:
    def __init__(self, ref, shape):
        self.ref, self.shape = ref, shape

    @property
    def at(self):
        return self

    def __getitem__(self, idx):
        idx = idx if isinstance(idx, tuple) else (idx,)
        flat = 0
        for i, n in zip(idx, self.shape):
            flat = flat * n + i
        return self.ref.at[flat]


def _flat_sem_types(rider):
    return tuple(pltpu.SemaphoreType.DMA((int(np.prod(s.shape)),)) for s in rider.scratch)


def _as_rider_sems(rider, refs):
    return [_FlatSems(r, s.shape) for r, s in zip(refs, rider.scratch)]


def _split_start(rider, name, after=()):
    n_in, n_out, n_sem = len(rider.operands), len(rider.out_shapes), len(rider.scratch)
    n_after = len(after)
    fresh = [j for j in range(n_out) if j not in rider.aliases.values()]
    by_out = {j: i for i, j in rider.aliases.items()}

    def body(*refs):
        r_in = refs[:n_in]
        refs = refs[n_in + n_after:]
        sems = refs[:n_sem]
        thru = refs[n_sem:n_sem + n_in]
        fresh_refs = refs[n_sem + n_in:n_sem + n_in + len(fresh)]
        token = refs[-1]
        r_out = [thru[by_out[j]] if j in by_out else fresh_refs[fresh.index(j)] for j in range(n_out)]
        rider.start(r_in, r_out, _as_rider_sems(rider, sems))
        token[...] = jnp.zeros_like(token)

    res = pl.pallas_call(
        body, name=name,
        out_shape=_flat_sem_types(rider) + tuple(pltpu.HBM(o.shape, o.dtype) for o in rider.operands)
        + tuple(pltpu.HBM(rider.out_shapes[j].shape, rider.out_shapes[j].dtype) for j in fresh)
        + (jax.ShapeDtypeStruct((8, LANES), F32),),
        in_specs=(HBM,) * n_in + (ANY,) * n_after,
        out_specs=(SEM,) * n_sem + (HBM,) * (n_in + len(fresh)) + (pl.BlockSpec(memory_space=pltpu.VMEM),),
        input_output_aliases={i: n_sem + i for i in range(n_in)}, compiler_params=_IN_FLIGHT,
    )(*[pltpu.with_memory_space_constraint(o, pltpu.HBM) for o in rider.operands], *after)
    return (rider, res[:n_sem], res[n_sem:n_sem + n_in], res[n_sem + n_in:-1]), res[-1]


def _split_wait(handles, after, name):
    rider, sems, thru, fresh_arrays = handles
    n_in, n_out, n_sem = len(rider.operands), len(rider.out_shapes), len(rider.scratch)
    fresh = [j for j in range(n_out) if j not in rider.aliases.values()]
    by_out = {j: i for i, j in rider.aliases.items()}
    n_data = n_in + len(fresh)

    def body(*refs):
        r_in = refs[:n_in]
        fresh_refs = refs[n_in:n_data]
        sem_refs = refs[n_data:n_data + n_sem]
        r_out = [r_in[by_out[j]] if j in by_out else fresh_refs[fresh.index(j)] for j in range(n_out)]
        rider.wait(r_in, r_out, _as_rider_sems(rider, sem_refs))

    data = list(thru) + list(fresh_arrays)
    res = pl.pallas_call(
        body, name=name, out_shape=tuple(pltpu.HBM(d.shape, d.dtype) for d in data),
        in_specs=(HBM,) * n_data + (SEM,) * n_sem + (ANY,) * len(after), out_specs=(HBM,) * n_data,
        input_output_aliases={i: i for i in range(n_data)}, compiler_params=_IN_FLIGHT,
    )(*data, *sems, *after)
    return [res[by_out[j]] if j in by_out else res[n_in + fresh.index(j)] for j in range(n_out)], res[:n_in]


def _pair_gather_rider(bufs):
    n = len(bufs)

    def copies(r_out, sems):
        x, y, c = _mesh_pos()
        out = []
        for a in range(n):
            send = functools.partial(
                    pltpu.make_async_remote_copy,
                src_ref=r_out[a].at[c], dst_ref=r_out[a].at[c], send_sem=sems[0].at[a],
                recv_sem=sems[1].at[a], device_id=(x, y, 1 - c), device_id_type=MESH)
            recv = functools.partial(
                    pltpu.make_async_remote_copy,
                src_ref=r_out[a].at[1 - c], dst_ref=r_out[a].at[1 - c], send_sem=sems[0].at[a],
                recv_sem=sems[1].at[a], device_id=(x, y, 1 - c), device_id_type=MESH)
            out.append((send, recv))
        return out

    def start(r_in, r_out, sems):
        for send, _ in copies(r_out, sems):
            send().start()

    def wait(r_in, r_out, sems):
        cps = copies(r_out, sems)
        for _, recv in cps:
            recv().wait_recv()
        for send, _ in cps:
            send().wait_send()

    return _Rider(bufs, [jax.ShapeDtypeStruct(b.shape, b.dtype) for b in bufs], {i: i for i in range(n)},
                  [pltpu.SemaphoreType.DMA((n,)), pltpu.SemaphoreType.DMA((n,))], start, wait)


def _add_own_half(g, recv, pos_arr, name):
    nb, rh, cols = g.shape[0], g.shape[-2], g.shape[-1]

    def body(pos_ref, g_ref, r_ref, send_ref, own_ref):
        s = (g_ref[...] + r_ref[...]).astype(BF16)
        send_ref[...] = s

        @pl.when(pl.program_id(0) == pos_ref[1])
        def _():
            own_ref[...] = s

    blk = pl.BlockSpec((None, rh, cols), lambda j, pos_ref: (j, 0, 0))
    g_spec = blk if g.ndim == 3 else pl.BlockSpec((None, None, rh, cols),
                                                   lambda j, pos_ref: (j, pos_ref[0], 0, 0))
    shape = jax.ShapeDtypeStruct((nb, rh, cols), BF16)
    return pl.pallas_call(
        body, name=name,
        grid_spec=pltpu.PrefetchScalarGridSpec(
            num_scalar_prefetch=1, grid=(nb,), in_specs=[g_spec, blk],
            out_specs=[blk, pl.BlockSpec((None, rh, cols), lambda j, pos_ref: (pos_ref[1], 0, 0))]),
        out_shape=[shape, shape], compiler_params=_params(),
    )(pos_arr, g, recv)


def _sum_chips(gath, pos_arr, name):
    nb, rh, cols = gath.shape

    def body(pos_ref, a_ref, b_ref, c_ref, d_ref, o_ref):
        del pos_ref
        o_ref[...] = ((a_ref[...].astype(F32) + b_ref[...].astype(F32)) + c_ref[...].astype(F32)) \
            + d_ref[...].astype(F32)

    tr = rh // 2 if (rh // 2) % 16 == 0 else rh
    specs = [pl.BlockSpec((None, tr, cols), functools.partial(lambda i, pos_ref, j: (j, i, 0), j=j))
             for j in range(nb)]
    return pl.pallas_call(
        body, name=name,
        grid_spec=pltpu.PrefetchScalarGridSpec(
            num_scalar_prefetch=1, grid=(rh // tr,), in_specs=specs,
            out_specs=pl.BlockSpec((None, tr, cols), lambda i, pos_ref: (pos_ref[0], i, 0))),
        out_shape=jax.ShapeDtypeStruct((2, rh, cols), F32), compiler_params=_params(),
    )(pos_arr, gath, gath, gath, gath)


def _small_allreduce(v, name, rider=None, after=()):
    n = v.shape[0]
    n_dev = 8

    def body(v_ref, o_ref, buf, send_sems, recv_sems):
        x, y, c = _mesh_pos()
        me = 4 * x + 2 * y + c
        buf[me] = v_ref[...]
        sends = []
        peers = []
        for r in range(1, n_dev):
            px = 1 - x if r & 4 else x
            py = 1 - y if r & 2 else y
            pc = 1 - c if r & 1 else c
            peers.append((px, py, pc))
            cp = pltpu.make_async_remote_copy(
                src_ref=v_ref, dst_ref=buf.at[me], send_sem=send_sems.at[r - 1],
                recv_sem=recv_sems.at[r - 1], device_id=(px, py, pc), device_id_type=MESH)
            cp.start()
            sends.append(cp)
        for r, (px, py, pc) in enumerate(peers):
            pltpu.make_async_remote_copy(
                src_ref=v_ref, dst_ref=buf.at[4 * px + 2 * py + pc], send_sem=send_sems.at[r],
                recv_sem=recv_sems.at[r], device_id=(px, py, pc), device_id_type=MESH).wait_recv()
        for cp in sends:
            cp.wait_send()
        acc = buf[0]
        for i in range(1, n_dev):
            acc = acc + buf[i]
        o_ref[...] = acc

    whole = pl.BlockSpec(v.shape, lambda i: (0, 0))
    return _pallas(
        body, name=name, grid=(1,), in_specs=[whole], out_specs=whole,
        out_shape=jax.ShapeDtypeStruct(v.shape, v.dtype), operands=[v],
        scratch_shapes=[pltpu.VMEM((n_dev, n, LANES), F32), pltpu.SemaphoreType.DMA((n_dev - 1,)),
                        pltpu.SemaphoreType.DMA((n_dev - 1,))],
        rider=rider, after=after)


def _adamw_math(w, g, m, v):
    m = ADAM_B1 * m + (1.0 - ADAM_B1) * g
    v = ADAM_B2 * v + (1.0 - ADAM_B2) * (g * g)
    m_hat = m / (1.0 - ADAM_B1 ** ADAM_STEP)
    v_hat = v / (1.0 - ADAM_B2 ** ADAM_STEP)
    delta = -ADAM_LR * (m_hat / (jnp.sqrt(v_hat) + ADAM_EPS) + ADAM_WD * w)
    return delta, m, v


def _adamw(w, g, m, v, name, after=()):
    r, c = w.shape
    tr = 128 if r % 128 == 0 else 64
    assert r % tr == 0

    def body(w_ref, g_ref, m_ref, v_ref, go_ref, d_ref, mo_ref, vo_ref):
        gv = g_ref[...]
        d, mn, vn = _adamw_math(w_ref[...], gv, m_ref[...], v_ref[...])
        go_ref[...] = gv
        d_ref[...] = d
        mo_ref[...] = mn
        vo_ref[...] = vn

    blk = pl.BlockSpec((tr, c), lambda i: (i, 0))
    return _pallas(body, name=name, grid=(r // tr,), in_specs=[blk] * 4, out_specs=[blk] * 4,
                   out_shape=[jax.ShapeDtypeStruct((r, c), F32)] * 4, operands=[w, g, m, v], after=after)


def _adamw_small(ws, gs, ms, vs, name):
    n = len(ws)

    def body(*refs):
        w_r, g_r, m_r, v_r = refs[:n], refs[n:2 * n], refs[2 * n:3 * n], refs[3 * n:4 * n]
        d_o, m_o, v_o = refs[4 * n:5 * n], refs[5 * n:6 * n], refs[6 * n:7 * n]
        for i in range(n):
            d, mn, vn = _adamw_math(w_r[i][...], g_r[i][...], m_r[i][...], v_r[i][...])
            d_o[i][...] = d
            m_o[i][...] = mn
            v_o[i][...] = vn

    specs = [pl.BlockSpec(w.shape, lambda i: (0, 0)) for w in ws]
    shapes = [jax.ShapeDtypeStruct(w.shape, F32) for w in ws]
    outs = pl.pallas_call(
        body, name=name, grid=(1,), in_specs=specs * 4, out_specs=specs * 3, out_shape=shapes * 3,
        compiler_params=_params(),
    )(*ws, *gs, *ms, *vs)
    return outs[:n], outs[n:2 * n], outs[2 * n:]


BIG = ("w_in", "w_o_attn", "w_pw_conv", "w_out", "w_ffn_in", "w_ffn_out")
ROW_SHARDED = ("w_out", "w_ffn_out")
SMALL = ("norm1_w", "b_gate", "q_norm_w", "k_norm_w", "conv_w", "conv_b", "conv_ln_w", "conv_ln_b", "norm2_w")
ORDER = ("norm1_w", "w_in", "b_gate", "q_norm_w", "k_norm_w", "w_o_attn", "conv_w", "conv_b", "conv_ln_w",
         "conv_ln_b", "w_pw_conv", "w_out", "norm2_w", "w_ffn_in", "w_ffn_out")
PACK_TILE = 8 * LANES


def _pack_small(parts):
    rows = []
    for p in parts:
        flat = p.reshape(-1)
        pad = (-flat.shape[0]) % PACK_TILE
        rows.append(jnp.pad(flat, (0, pad)).reshape(-1, LANES))
    return jnp.concatenate(rows, axis=0)


def _unpack_small(packed, shapes):
    out, row = [], 0
    for shp in shapes:
        size = int(np.prod(shp))
        nrow = -(-size // PACK_TILE) * (PACK_TILE // LANES)
        out.append(packed[row:row + nrow].reshape(-1)[:size].reshape(shp))
        row += nrow
    return out


def kernel(x, positions, norm1_w, w_in, b_gate, q_norm_w, k_norm_w, w_o_attn, conv_w, conv_b, conv_ln_w, conv_ln_b, w_pw_conv, w_out, norm2_w, w_ffn_in, w_ffn_out, loss_target, m_norm1_w, m_w_in, m_b_gate, m_q_norm_w, m_k_norm_w, m_w_o_attn, m_conv_w, m_conv_b, m_conv_ln_w, m_conv_ln_b, m_w_pw_conv, m_w_out, m_norm2_w, m_w_ffn_in, m_w_ffn_out, v_norm1_w, v_w_in, v_b_gate, v_q_norm_w, v_k_norm_w, v_w_o_attn, v_conv_w, v_conv_b, v_conv_ln_w, v_conv_ln_b, v_w_pw_conv, v_w_out, v_norm2_w, v_w_ffn_in, v_w_ffn_out):
    w = dict(norm1_w=norm1_w, w_in=w_in, b_gate=b_gate, q_norm_w=q_norm_w, k_norm_w=k_norm_w, w_o_attn=w_o_attn,
             conv_w=conv_w, conv_b=conv_b, conv_ln_w=conv_ln_w, conv_ln_b=conv_ln_b, w_pw_conv=w_pw_conv,
             w_out=w_out, norm2_w=norm2_w, w_ffn_in=w_ffn_in, w_ffn_out=w_ffn_out)
    m = dict(norm1_w=m_norm1_w, w_in=m_w_in, b_gate=m_b_gate, q_norm_w=m_q_norm_w, k_norm_w=m_k_norm_w,
             w_o_attn=m_w_o_attn, conv_w=m_conv_w, conv_b=m_conv_b, conv_ln_w=m_conv_ln_w,
             conv_ln_b=m_conv_ln_b, w_pw_conv=m_w_pw_conv, w_out=m_w_out, norm2_w=m_norm2_w,
             w_ffn_in=m_w_ffn_in, w_ffn_out=m_w_ffn_out)
    v = dict(norm1_w=v_norm1_w, w_in=v_w_in, b_gate=v_b_gate, q_norm_w=v_q_norm_w, k_norm_w=v_k_norm_w,
             w_o_attn=v_w_o_attn, conv_w=v_conv_w, conv_b=v_conv_b, conv_ln_w=v_conv_ln_w,
             conv_ln_b=v_conv_ln_b, w_pw_conv=v_w_pw_conv, w_out=v_w_out, norm2_w=v_norm2_w,
             w_ffn_in=v_w_ffn_in, w_ffn_out=v_w_ffn_out)
    cx, cy, cc = _mesh_pos()
    chip = 2 * cx + cy

    chip_arr = chip.reshape(1).astype(jnp.int32)
    pos_arr = jnp.stack([cc, chip]).astype(jnp.int32)
    bufs = {}
    for n in BIG:
        buf = _cast_into_slot(w[n][0], chip_arr, BF16, f"cast_{n}")
        bufs[n] = buf.reshape(N_CHIPS, 2, buf.shape[1] // 2, buf.shape[2])
    small_bufs = [_cast_into_slot(w[n][0], chip_arr, F32, f"slot_{n}") for n in ("conv_w", "b_gate")]
    w_in_buf, conv_w_buf, b_gate_buf = _comm_call(_gather_both_legs_rider([bufs["w_in"]], small_bufs),
                                                  "allgather_w_in")
    wts = dict(w_in=w_in_buf.reshape(N_CHIPS, -1, w_in_buf.shape[3]),
               conv_w=conv_w_buf.transpose(1, 0, 2).reshape(CONV_WIDTH, -1),
               b_gate=b_gate_buf.transpose(1, 0, 2).reshape(2, 1, -1),
               norm1_w=norm1_w, q_norm_w=q_norm_w, k_norm_w=k_norm_w, conv_b=conv_b, conv_ln_w=conv_ln_w,
               conv_ln_b=conv_ln_b, norm2_w=norm2_w)

    loss, grad_x, g, reduced, w_in_in_flight = _forward_backward(
        x[0], positions.reshape(-1, 1), loss_target[0], wts, [bufs[n] for n in LATE_GATHER], pos_arr)
    grads = {n: b.reshape(-1, b.shape[2]) for n, b in reduced.items()}

    w_in_in_flight, started = w_in_in_flight
    delta, new_m, new_v = {}, {}, {}
    for n in EARLY_REDUCE:
        grads[n], delta[n], new_m[n], new_v[n] = _adamw(w[n][0], grads[n], m[n][0], v[n][0], f"adamw_{n}",
                                                        after=[started])
    small_parts = [loss] + [g[n] for n in SMALL]
    small_shapes = [p.shape for p in small_parts]
    summed = _small_allreduce(_pack_small(small_parts), "small_allreduce",
                              after=[delta[n] for n in EARLY_REDUCE])
    reduced = _unpack_small(summed, small_shapes)
    loss_total = reduced[0].reshape(())
    for n, r in zip(SMALL, reduced[1:]):
        grads[n] = r
    ch_shard = conv_w.shape[2]
    grads["conv_w"] = lax.dynamic_slice_in_dim(grads["conv_w"], chip * ch_shard, ch_shard, axis=1)
    d_shard = b_gate.shape[2]
    grads["b_gate"] = lax.dynamic_slice_in_dim(grads["b_gate"], chip * d_shard, d_shard, axis=1)

    (by_chip_w_in,), _ = _split_wait(w_in_in_flight, after=[delta[n] for n in EARLY_REDUCE] + [summed],
                                     name="grads_w_in_exchange_wait")
    half_w_in = _sum_chips(by_chip_w_in, pos_arr, "grads_chip_sum_w_in")
    (shard_w_in,) = _comm_call(_pair_gather_rider([half_w_in]), "grads_pair_gather_w_in")
    grads["w_in"], delta["w_in"], new_m["w_in"], new_v["w_in"] = _adamw(
        w["w_in"][0], shard_w_in.reshape(-1, shard_w_in.shape[2]), m["w_in"][0], v["w_in"][0], "adamw_w_in")
    flat2 = lambda a: a.reshape(-1, a.shape[-1])
    d_s, m_s, v_s = _adamw_small([flat2(w[n]) for n in SMALL], [flat2(grads[n]) for n in SMALL],
                                 [flat2(m[n]) for n in SMALL], [flat2(v[n]) for n in SMALL], "adamw_small")
    for i, n in enumerate(SMALL):
        delta[n], new_m[n], new_v[n] = d_s[i], m_s[i], v_s[i]

    shaped = lambda d, n: d[n].reshape(w[n].shape)
    return (loss_total, grad_x[None], *[shaped(grads, n) for n in ORDER], *[shaped(delta, n) for n in ORDER],
            *[shaped(new_m, n) for n in ORDER], *[shaped(new_v, n) for n in ORDER])
```

```python
import functools

import numpy as np
import jax
import jax.numpy as jnp
from jax import lax
from jax.experimental import pallas as pl
from jax.experimental.pallas import tpu as pltpu

F32 = jnp.float32
BF16 = jnp.bfloat16
MESH = pl.DeviceIdType.MESH
ANY = pl.BlockSpec(memory_space=pl.ANY)

HEAD_DIM = 64
N_SLOT_HEADS = 8
DILATIONS = (1, 4, 16)
HALF_SPAN = 64
ROPE_THETA = 500000.0
ROT_DIM = 16
CONV_WIDTH = 31
EPS = 1e-6
NEG_INF = -1e30
ADAM_LR, ADAM_B1, ADAM_B2, ADAM_EPS, ADAM_WD, ADAM_STEP = 0.001, 0.9, 0.999, 1e-08, 0.01, 10

LANES = 128
QBLK = 128
KWIN = QBLK + 2 * HALF_SPAN
VMEM_LIMIT = 48 * 1024 * 1024
N_CHIPS = 4


def _params(**kw):
    return pltpu.CompilerParams(vmem_limit_bytes=VMEM_LIMIT, **kw)


class _Rider:
    def __init__(self, operands, out_shapes, aliases, scratch, start, wait):
        self.operands, self.out_shapes, self.aliases = list(operands), list(out_shapes), dict(aliases)
        self.scratch, self.start, self.wait = list(scratch), start, wait


def _pallas(body, *, name, grid, in_specs, out_specs, out_shape, operands, scratch_shapes=(), aliases=None,
            rider=None, after=()):
    single = not isinstance(out_specs, (list, tuple))
    out_specs_l = [out_specs] if single else list(out_specs)
    out_shape_l = [out_shape] if single else list(out_shape)
    aliases = dict(aliases or {})

    def call(fn, all_in_specs, all_out_specs, all_out_shape, all_scratch, all_aliases, all_operands):
        return pl.pallas_call(
            fn, name=name, grid=grid, in_specs=all_in_specs, out_specs=all_out_specs, out_shape=all_out_shape,
            scratch_shapes=all_scratch, input_output_aliases=all_aliases, compiler_params=_params(),
        )(*all_operands)

    if rider is None:
        n_main = len(in_specs)

        def ordered(*refs):
            body(*refs[:n_main], *refs[n_main + len(after):])

        res = call(ordered if after else body, list(in_specs) + [ANY] * len(after), out_specs_l, out_shape_l,
                   list(scratch_shapes), aliases, list(operands) + list(after))
        return res[0] if single else res
    assert not after
    n_in, n_rin = len(in_specs), len(rider.operands)
    n_out, n_rout = len(out_specs_l), len(rider.out_shapes)
    n_sc = len(scratch_shapes)

    def wrapped(*refs):
        main_in, r_in = refs[:n_in], refs[n_in:n_in + n_rin]
        o0 = n_in + n_rin
        main_out, r_out = refs[o0:o0 + n_out], refs[o0 + n_out:o0 + n_out + n_rout]
        s0 = o0 + n_out + n_rout
        main_sc, r_sc = refs[s0:s0 + n_sc], refs[s0 + n_sc:]
        ids = [pl.program_id(d) for d in range(len(grid))]
        first = functools.reduce(jnp.logical_and, [i == 0 for i in ids])
        last = functools.reduce(jnp.logical_and, [i == n - 1 for i, n in zip(ids, grid)])

        @pl.when(first)
        def _():
            rider.start(r_in, r_out, r_sc)

        body(*main_in, *main_out, *main_sc)

        @pl.when(last)
        def _():
            rider.wait(r_in, r_out, r_sc)

    for src, dst in rider.aliases.items():
        aliases[n_in + src] = n_out + dst
    res = call(wrapped, list(in_specs) + [ANY] * n_rin, out_specs_l + [ANY] * n_rout,
               out_shape_l + rider.out_shapes, list(scratch_shapes) + rider.scratch, aliases,
               list(operands) + rider.operands)
    main = res[:n_out]
    return (main[0] if single else main), res[n_out:]


def _matmul(a, b, *, mode, tm, tn, tk, out_dtype, name, b_blocked=False,
            out_blocked=None, cols_outer=False, rider=None, after=()):
    a_shape = a.shape
    if mode == "nn":
        m_dim, k_dim = a_shape
        n_dim = b.shape[0] * b.shape[2] if b_blocked else b.shape[1]
        rows, cols, red = m_dim, n_dim, k_dim
    elif mode == "nt":
        m_dim, n_dim = a_shape
        k_dim = b.shape[1] if b_blocked else b.shape[0]
        rows, cols, red = m_dim, k_dim, n_dim
    else:
        m_dim, k_dim = a_shape
        n_dim = b.shape[1]
        rows, cols, red = k_dim, n_dim, m_dim
    assert rows % tm == 0 and cols % tn == 0 and red % tk == 0, (name, rows, cols, red)
    ni, nj, nk = rows // tm, cols // tn, red // tk

    if mode == "nn":
        a_spec = pl.BlockSpec((tm, tk), lambda i, j, k: (i, k))
        if b_blocked:
            per = b.shape[2] // tn
            b_spec = pl.BlockSpec((None, tk, tn), lambda i, j, k: (j // per, k, j % per))
        else:
            b_spec = pl.BlockSpec((tk, tn), lambda i, j, k: (k, j))
        dims = (((1,), (0,)), ((), ()))
    elif mode == "nt":
        a_spec = pl.BlockSpec((tm, tk), lambda i, j, k: (i, k))
        if b_blocked:
            per = b.shape[2] // tk
            b_spec = pl.BlockSpec((None, tn, tk), lambda i, j, k: (k // per, j, k % per))
        else:
            b_spec = pl.BlockSpec((tn, tk), lambda i, j, k: (j, k))
        dims = (((1,), (1,)), ((), ()))
    else:
        a_spec = pl.BlockSpec((tk, tm), lambda i, j, k: (k, i))
        b_spec = pl.BlockSpec((tk, tn), lambda i, j, k: (k, j))
        dims = (((0,), (0,)), ((), ()))

    if out_blocked:
        per_o = (cols // out_blocked) // tn
        out_spec = pl.BlockSpec((None, tm, tn), lambda i, j, k: (j // per_o, i, j % per_o))
        out_shape = jax.ShapeDtypeStruct((out_blocked, rows, cols // out_blocked), out_dtype)
    else:
        out_spec = pl.BlockSpec((tm, tn), lambda i, j, k: (i, j))
        out_shape = jax.ShapeDtypeStruct((rows, cols), out_dtype)

    def body(a_ref, b_ref, o_ref, *acc):
        prod = lax.dot_general(a_ref[...], b_ref[...], dims, preferred_element_type=F32)
        if nk == 1:
            o_ref[...] = prod.astype(out_dtype)
        else:
            acc_ref, = acc
            k = pl.program_id(2)

            @pl.when(k == 0)
            def _():
                acc_ref[...] = prod

            @pl.when(k > 0)
            def _():
                acc_ref[...] += prod

            @pl.when(k == nk - 1)
            def _():
                o_ref[...] = acc_ref[...].astype(out_dtype)

    scratch = [pltpu.VMEM((tm, tn), F32)] if nk > 1 else []
    grid = (ni, nj, nk)
    if cols_outer:
        swap = lambda spec: pl.BlockSpec(spec.block_shape, lambda j, i, k, f=spec.index_map: f(i, j, k))
        a_spec, b_spec, out_spec, grid = swap(a_spec), swap(b_spec), swap(out_spec), (nj, ni, nk)
    return _pallas(body, name=name, grid=grid, in_specs=[a_spec, b_spec], out_specs=out_spec,
                   out_shape=out_shape, operands=[a, b], scratch_shapes=scratch, rider=rider, after=after)


def _rmsnorm_fwd(x, w, name):
    s, d = x.shape
    tm = 256

    def body(x_ref, w_ref, o_ref):
        xv = x_ref[...]
        rstd = lax.rsqrt(jnp.mean(xv * xv, axis=-1, keepdims=True) + EPS)
        o_ref[...] = (xv * rstd * w_ref[...]).astype(BF16)

    return pl.pallas_call(
        body, name=name, grid=(s // tm,),
        in_specs=[pl.BlockSpec((tm, d), lambda i: (i, 0)), pl.BlockSpec((1, d), lambda i: (0, 0))],
        out_specs=pl.BlockSpec((tm, d), lambda i: (i, 0)),
        out_shape=jax.ShapeDtypeStruct((s, d), BF16), compiler_params=_params(),
    )(x, w)


def _rmsnorm_bwd(dh, x, w, dres, name, rider=None):
    s, d = x.shape
    tm = 256

    def body(dh_ref, x_ref, w_ref, dres_ref, dx_ref, dxb_ref, dw_ref):
        xv = x_ref[...]
        rstd = lax.rsqrt(jnp.mean(xv * xv, axis=-1, keepdims=True) + EPS)
        xhat = xv * rstd
        dhv = dh_ref[...]
        g = dhv * w_ref[...]
        dx = rstd * (g - xhat * jnp.mean(g * xhat, axis=-1, keepdims=True)) + dres_ref[...]
        dx_ref[...] = dx
        dxb_ref[...] = dx.astype(BF16)
        part = jnp.sum(dhv * xhat, axis=0, keepdims=True)

        @pl.when(pl.program_id(0) == 0)
        def _():
            dw_ref[...] = part

        @pl.when(pl.program_id(0) > 0)
        def _():
            dw_ref[...] += part

    row = pl.BlockSpec((tm, d), lambda i: (i, 0))
    vec = pl.BlockSpec((1, d), lambda i: (0, 0))
    return _pallas(
        body, name=name, grid=(s // tm,), in_specs=[row, row, vec, row], out_specs=[row, row, vec],
        out_shape=[jax.ShapeDtypeStruct((s, d), F32), jax.ShapeDtypeStruct((s, d), BF16),
                   jax.ShapeDtypeStruct((1, d), F32)],
        operands=[dh, x, w, dres], rider=rider)


def _rope_consts():
    lane = np.arange(LANES)
    in_head = lane % HEAD_DIM
    inv_freq = ROPE_THETA ** (-jnp.arange(0, ROT_DIM, 2, dtype=F32) / ROT_DIM)
    invf = jnp.where(jnp.asarray(in_head < ROT_DIM), jnp.tile(inv_freq, LANES // (ROT_DIM // 2)), 0.0)
    m_a = np.where(in_head < ROT_DIM // 2, -1.0, 0.0).astype(np.float32)
    m_b = np.where((in_head >= ROT_DIM // 2) & (in_head < ROT_DIM), 1.0, 0.0).astype(np.float32)
    block_diag = (lane[:, None] // HEAD_DIM == lane[None, :] // HEAD_DIM).astype(np.float32)
    return (invf.reshape(1, LANES).astype(F32), jnp.asarray(m_a).reshape(1, LANES),
            jnp.asarray(m_b).reshape(1, LANES), jnp.asarray(block_diag, dtype=BF16))


def _head_sums(v, bd):
    hi = v.astype(BF16)
    lo = (v - hi.astype(F32)).astype(BF16)
    return jnp.dot(hi, bd, preferred_element_type=F32) + jnp.dot(lo, bd, preferred_element_type=F32)


def _qk_fwd(proj, pos_col, qw2, kw2, consts, name, rider=None):
    s = proj.shape[0]
    width = 3 * N_SLOT_HEADS * HEAD_DIM
    tm = 128
    invf, m_a, m_b, bd = consts
    scale = HEAD_DIM ** -0.5

    def body(q_ref, k_ref, pos_ref, qw_ref, kw_ref, invf_ref, ma_ref, mb_ref, bd_ref, qo_ref, ko_ref):
        ang = pos_ref[...].astype(F32) * invf_ref[...]
        cos = jnp.cos(ang)
        sin = jnp.sin(ang)
        s_a = sin * ma_ref[...]
        s_b = sin * mb_ref[...]
        bdv = bd_ref[...]
        for src, w_ref, dst, sc in ((q_ref, qw_ref, qo_ref, scale), (k_ref, kw_ref, ko_ref, 1.0)):
            for cb in range(width // LANES):
                cols = slice(cb * LANES, (cb + 1) * LANES)
                t = src[:, cols]
                rstd = lax.rsqrt(_head_sums(t * t, bdv) * (1.0 / HEAD_DIM) + EPS)
                y = t * rstd * w_ref[...]
                r = y * cos + pltpu.roll(y, LANES - 8, axis=1) * s_a + pltpu.roll(y, 8, axis=1) * s_b
                dst[:, cols] = r * sc if sc != 1.0 else r

    vec = pl.BlockSpec((1, LANES), lambda i: (0, 0))
    return _pallas(
        body, name=name, grid=(s // tm,),
        in_specs=[pl.BlockSpec((tm, width), lambda i: (i, 0)), pl.BlockSpec((tm, width), lambda i: (i, 1)),
                  pl.BlockSpec((tm, 1), lambda i: (i, 0)), vec, vec, vec, vec, vec,
                  pl.BlockSpec((LANES, LANES), lambda i: (0, 0))],
        out_specs=[pl.BlockSpec((tm, width), lambda i: (i, 0))] * 2,
        out_shape=[jax.ShapeDtypeStruct((s, width), F32)] * 2,
        operands=[proj, proj, pos_col, qw2, kw2, invf, m_a, m_b, bd], rider=rider)


def _qk_bwd(dqn, dkn, dv, da, db, dgl, proj, pos_col, qw2, kw2, consts, name, rider=None):
    s = proj.shape[0]
    width = 3 * N_SLOT_HEADS * HEAD_DIM
    ch = da.shape[1]
    gate_w = dgl.shape[1]
    out_w = 3 * width + 2 * ch + gate_w
    assert out_w == proj.shape[1]
    tm = 128
    invf, m_a, m_b, bd = consts
    scale = HEAD_DIM ** -0.5

    def body(dq_ref, dk_ref, dv_ref, da_ref, db_ref, dgl_ref, q_ref, k_ref, pos_ref, qw_ref, kw_ref,
             invf_ref, ma_ref, mb_ref, bd_ref, out_ref, dqw_ref, dkw_ref):
        ang = pos_ref[...].astype(F32) * invf_ref[...]
        cos = jnp.cos(ang)
        sin = jnp.sin(ang)
        s_a = sin * ma_ref[...]
        s_b = sin * mb_ref[...]
        bdv = bd_ref[...]
        first = pl.program_id(0) == 0
        for src, dsrc, w_ref, col0, dw_ref, sc in ((q_ref, dq_ref, qw_ref, 0, dqw_ref, scale),
                                                   (k_ref, dk_ref, kw_ref, width, dkw_ref, 1.0)):
            dw_acc = jnp.zeros((1, LANES), F32)
            for cb in range(width // LANES):
                cols = slice(cb * LANES, (cb + 1) * LANES)
                t = src[:, cols]
                dr = dsrc[:, cols]
                if sc != 1.0:
                    dr = dr * sc
                dy = dr * cos + pltpu.roll(dr * s_a, 8, axis=1) + pltpu.roll(dr * s_b, LANES - 8, axis=1)
                rstd = lax.rsqrt(_head_sums(t * t, bdv) * (1.0 / HEAD_DIM) + EPS)
                xhat = t * rstd
                g = dy * w_ref[...]
                dt = rstd * (g - xhat * (_head_sums(g * xhat, bdv) * (1.0 / HEAD_DIM)))
                out_ref[:, col0 + cb * LANES: col0 + (cb + 1) * LANES] = dt.astype(BF16)
                dw_acc = dw_acc + jnp.sum(dy * xhat, axis=0, keepdims=True)
            dw_acc = dw_acc + pltpu.roll(dw_acc, HEAD_DIM, axis=1)

            @pl.when(first)
            def _(dw_ref=dw_ref, dw_acc=dw_acc):
                dw_ref[...] = dw_acc

            @pl.when(jnp.logical_not(first))
            def _(dw_ref=dw_ref, dw_acc=dw_acc):
                dw_ref[...] += dw_acc
        out_ref[:, 2 * width: 3 * width] = dv_ref[...].astype(BF16)
        out_ref[:, 3 * width: 3 * width + ch] = da_ref[...]
        out_ref[:, 3 * width + ch: 3 * width + 2 * ch] = db_ref[...]
        out_ref[:, 3 * width + 2 * ch: out_w] = dgl_ref[...]

    vec = pl.BlockSpec((1, LANES), lambda i: (0, 0))
    blk = lambda c: pl.BlockSpec((tm, width), lambda i: (i, c))
    cblk = pl.BlockSpec((tm, ch), lambda i: (i, 0))
    return _pallas(
        body, name=name, grid=(s // tm,),
        in_specs=[blk(0), blk(0), blk(0), cblk, cblk, pl.BlockSpec((tm, gate_w), lambda i: (i, 0)),
                  blk(0), blk(1), pl.BlockSpec((tm, 1), lambda i: (i, 0)), vec, vec, vec, vec, vec,
                  pl.BlockSpec((LANES, LANES), lambda i: (0, 0))],
        out_specs=[pl.BlockSpec((tm, out_w), lambda i: (i, 0)), vec, vec],
        out_shape=[jax.ShapeDtypeStruct((s, out_w), BF16)] + [jax.ShapeDtypeStruct((1, LANES), F32)] * 2,
        operands=[dqn, dkn, dv, da, db, dgl, proj, proj, pos_col, qw2, kw2, invf, m_a, m_b, bd],
        rider=rider)


def _row_chunks(n_rows, fn, chunk=256):
    def step(i, c):
        fn(pl.ds(pl.multiple_of(i * chunk, chunk), chunk))
        return c
    lax.fori_loop(0, n_rows // chunk, step, 0)


def _to_residue_major(dst, src, s, d, dst_off=0, cast=None):
    seq = s // d
    for r in range(d):
        v = src[...] if d == 1 else src[pl.ds(r, seq, stride=d), :]
        dst[dst_off + r * seq: dst_off + (r + 1) * seq, :] = v if cast is None else v.astype(cast)


def _from_residue_major(dst, src, s, d, src_off=0):
    seq = s // d
    for r in range(d):
        v = src[src_off + r * seq: src_off + (r + 1) * seq, :]
        if d == 1:
            dst[...] = v
        else:
            dst[pl.ds(r, seq, stride=d), :] = v


def _band_bias():
    qi = lax.broadcasted_iota(jnp.int32, (QBLK, KWIN), 0)
    kj = lax.broadcasted_iota(jnp.int32, (QBLK, KWIN), 1)
    return jnp.where(jnp.abs(kj - HALF_SPAN - qi) <= HALF_SPAN, 0.0, NEG_INF).astype(F32)


def _range_bias(base, seq):
    kj = lax.broadcasted_iota(jnp.int32, (1, KWIN), 1)
    lo = (base & -seq) - base + HALF_SPAN
    return jnp.where((kj >= lo) & (kj < lo + seq), 0.0, NEG_INF).astype(F32)


def _skewed_blocks(n_blk, produce, consume):
    produce(0, 0)
    for b in range(n_blk):
        consume(b, b % 2)
        if b + 1 < n_blk:
            produce(b + 1, (b + 1) % 2)


def _block_base(b):
    return b * QBLK if isinstance(b, int) else pl.multiple_of(b * QBLK, QBLK)


def _attn_fwd(qn, kn, proj, name, rider=None):
    s = qn.shape[0]
    n_pairs = N_SLOT_HEADS * HEAD_DIM // LANES
    v_col0 = 2 * qn.shape[1] // LANES
    nt_dims = (((1,), (1,)), ((), ()))

    def body(q_ref, k_ref, v_ref, attn_ref, lse_ref, attn_b_ref, q_rm, k_rm, v_rm, acc_rm, m_rm, l_rm,
             acc_p, m_p, l_p, m_run, l_run, acc_run, band, s_buf, m_buf):
        g = pl.program_id(1)
        zpad = jnp.zeros((HALF_SPAN, LANES), BF16)
        k_rm[0:HALF_SPAN, :] = zpad
        k_rm[s + HALF_SPAN: s + 2 * HALF_SPAN, :] = zpad
        v_rm[0:HALF_SPAN, 0:LANES] = zpad
        v_rm[s + HALF_SPAN: s + 2 * HALF_SPAN, 0:LANES] = zpad

        def ones_rows(rows):
            v_rm[pl.ds(rows.start, rows.size), LANES:2 * LANES] = jnp.ones((rows.size, LANES), BF16)

        _row_chunks(s + 2 * HALF_SPAN, ones_rows, chunk=2 * HALF_SPAN)
        band[...] = _band_bias()
        lane = lax.broadcasted_iota(jnp.int32, (QBLK, LANES), 1)
        low = lane < HEAD_DIM
        n_blk = s // QBLK

        for gi, d in enumerate(DILATIONS):
            @pl.when(g == gi)
            def _(gi=gi, d=d):
                seq = s // d
                _to_residue_major(q_rm, q_ref, s, d, cast=BF16)
                _to_residue_major(k_rm, k_ref, s, d, dst_off=HALF_SPAN, cast=BF16)
                _to_residue_major(v_rm.at[:, 0:LANES], v_ref, s, d, dst_off=HALF_SPAN, cast=BF16)

                def scores(b, slot):
                    base = _block_base(b)
                    q = q_rm[pl.ds(base, QBLK), :]
                    zero = jnp.zeros_like(q)
                    q2 = jnp.concatenate([jnp.where(low, q, zero), jnp.where(low, zero, q)], axis=0)
                    sc = lax.dot_general(q2, k_rm[pl.ds(base, KWIN), :], nt_dims, preferred_element_type=F32)
                    bias = band[...] + _range_bias(base, seq)
                    for hh in range(2):
                        rows = slice(hh * QBLK, (hh + 1) * QBLK)
                        sh = sc[rows, :] + bias
                        s_buf[slot, rows, :] = sh
                        m_buf[slot, rows, :] = jnp.broadcast_to(jnp.max(sh, axis=-1, keepdims=True), (QBLK, LANES))

                def outputs(b, slot):
                    base = _block_base(b)
                    sv = s_buf[slot]
                    mb = m_buf[slot]
                    p = jnp.exp(jnp.concatenate([sv[:, 0:LANES] - mb, sv[:, LANES:2 * LANES] - mb], axis=1))
                    pv = jnp.dot(p.astype(BF16), v_rm[pl.ds(base, KWIN), :], preferred_element_type=F32)
                    rows = pl.ds(base, QBLK)
                    acc_rm[rows, :] = jnp.where(low, pv[0:QBLK, 0:LANES], pv[QBLK:2 * QBLK, 0:LANES])
                    l_rm[rows, :] = jnp.where(low, pv[0:QBLK, LANES:2 * LANES], pv[QBLK:2 * QBLK, LANES:2 * LANES])
                    m_rm[rows, :] = jnp.where(low, mb[0:QBLK, :], mb[QBLK:2 * QBLK, :])

                _skewed_blocks(n_blk, scores, outputs)
                if d == 1:
                    src = (acc_rm, m_rm, l_rm)
                else:
                    for dst_, src_ in ((acc_p, acc_rm), (m_p, m_rm), (l_p, l_rm)):
                        _from_residue_major(dst_, src_, s, d)
                    src = (acc_p, m_p, l_p)

                def combine(rows):
                    a_g, m_g, l_g = src[0][rows, :], src[1][rows, :], src[2][rows, :]
                    if gi == 0:
                        m_new, l_new, a_new = m_g, l_g, a_g
                    else:
                        m_old = m_run[rows, :]
                        m_new = jnp.maximum(m_old, m_g)
                        w_old = jnp.exp(m_old - m_new)
                        w_g = jnp.exp(m_g - m_new)
                        l_new = l_run[rows, :] * w_old + l_g * w_g
                        a_new = acc_run[rows, :] * w_old + a_g * w_g
                    if gi == len(DILATIONS) - 1:
                        out = a_new / l_new
                        attn_ref[rows, :] = out
                        attn_b_ref[rows, :] = out.astype(BF16)
                        lse_ref[rows, :] = m_new + jnp.log(l_new)
                    else:
                        m_run[rows, :] = m_new
                        l_run[rows, :] = l_new
                        acc_run[rows, :] = a_new

                _row_chunks(s, combine)

    qk_spec = pl.BlockSpec((s, LANES), lambda hp, g: (0, g * n_pairs + hp))
    v_spec = pl.BlockSpec((s, LANES), lambda hp, g: (0, v_col0 + g * n_pairs + hp))
    o_spec = pl.BlockSpec((s, LANES), lambda hp, g: (0, hp))
    f32buf = pltpu.VMEM((s, LANES), F32)
    return _pallas(
        body, name=name, grid=(n_pairs, len(DILATIONS)), in_specs=[qk_spec, qk_spec, v_spec],
        out_specs=[o_spec, o_spec, o_spec],
        out_shape=[jax.ShapeDtypeStruct((s, n_pairs * LANES), F32)] * 2
        + [jax.ShapeDtypeStruct((s, n_pairs * LANES), BF16)],
        operands=[qn, kn, proj],
        scratch_shapes=[pltpu.VMEM((s, LANES), BF16), pltpu.VMEM((s + 2 * HALF_SPAN, LANES), BF16),
                        pltpu.VMEM((s + 2 * HALF_SPAN, 2 * LANES), BF16)] + [f32buf] * 9
        + [pltpu.VMEM((QBLK, KWIN), F32), pltpu.VMEM((2, 2 * QBLK, KWIN), F32),
           pltpu.VMEM((2, 2 * QBLK, LANES), F32)],
        rider=rider)


def _attn_bwd(qn, kn, proj, dattn, attn, lse, bd, name, rider=None):
    s = qn.shape[0]
    n_pairs = N_SLOT_HEADS * HEAD_DIM // LANES
    v_col0 = 2 * qn.shape[1] // LANES
    nt_dims = (((1,), (1,)), ((), ()))
    tn_dims = (((0,), (0,)), ((), ()))
    spad = s + 2 * HALF_SPAN

    def body(q_ref, k_ref, v_ref, do_ref, o_ref, lse_ref, bd_ref, dq_ref, dk_ref, dv_ref,
             q_rm, k_rm, v_rm, do_rm, lse0_rm, lse1_rm, dd0_rm, dd1_rm, dq_rm, dk_rm, dv_rm,
             lse0_p, lse1_p, dd0_p, dd1_p, band, p_buf, ds_buf):
        g = pl.program_id(1)
        zpad = jnp.zeros((HALF_SPAN, LANES), BF16)
        for buf in (k_rm, v_rm):
            buf[0:HALF_SPAN, :] = zpad
            buf[s + HALF_SPAN: spad, :] = zpad
        zf = jnp.zeros((HALF_SPAN, LANES), F32)
        for buf in (dk_rm, dv_rm):
            buf[0:HALF_SPAN, :] = zf
            buf[s + HALF_SPAN: spad, :] = zf
        band[...] = _band_bias()

        def clear(rows):
            z = jnp.zeros((rows.size, LANES), F32)
            dk_rm[pl.ds(rows.start + HALF_SPAN, rows.size), :] = z
            dv_rm[pl.ds(rows.start + HALF_SPAN, rows.size), :] = z

        _row_chunks(s, clear)

        def prepare(rows):
            lo = lax.broadcasted_iota(jnp.int32, (rows.size, LANES), 1) < HEAD_DIM
            dsum = _head_sums(do_ref[rows, :] * o_ref[rows, :], bd_ref[...])
            dswap = pltpu.roll(dsum, HEAD_DIM, axis=1)
            dd0_p[rows, :] = jnp.where(lo, dsum, dswap)
            dd1_p[rows, :] = jnp.where(lo, dswap, dsum)
            lv = lse_ref[rows, :]
            lswap = pltpu.roll(lv, HEAD_DIM, axis=1)
            lse0_p[rows, :] = jnp.where(lo, lv, lswap)
            lse1_p[rows, :] = jnp.where(lo, lswap, lv)

        @pl.when(g == 0)
        def _():
            _row_chunks(s, prepare)
        lane = lax.broadcasted_iota(jnp.int32, (QBLK, LANES), 1)
        low = lane < HEAD_DIM
        n_blk = s // QBLK

        def stacked(ref, rows):
            val = ref[rows, :]
            zero = jnp.zeros_like(val)
            return jnp.concatenate([jnp.where(low, val, zero), jnp.where(low, zero, val)], axis=0)

        for gi, d in enumerate(DILATIONS):
            @pl.when(g == gi)
            def _(d=d):
                seq = s // d
                _to_residue_major(q_rm, q_ref, s, d, cast=BF16)
                _to_residue_major(k_rm, k_ref, s, d, dst_off=HALF_SPAN, cast=BF16)
                _to_residue_major(v_rm, v_ref, s, d, dst_off=HALF_SPAN, cast=BF16)
                _to_residue_major(do_rm, do_ref, s, d, cast=BF16)
                for dst_, src_ in ((lse0_rm, lse0_p), (lse1_rm, lse1_p), (dd0_rm, dd0_p), (dd1_rm, dd1_p)):
                    _to_residue_major(dst_, src_, s, d)

                def scores(b, slot):
                    base = _block_base(b)
                    rows = pl.ds(base, QBLK)
                    win = pl.ds(base, KWIN)
                    sc = lax.dot_general(stacked(q_rm, rows), k_rm[win, :], nt_dims, preferred_element_type=F32)
                    dp = lax.dot_general(stacked(do_rm, rows), v_rm[win, :], nt_dims, preferred_element_type=F32)
                    bias = band[...] + _range_bias(base, seq)
                    for hh, (lse_r, dd_r) in enumerate(((lse0_rm, dd0_rm), (lse1_rm, dd1_rm))):
                        r = slice(hh * QBLK, (hh + 1) * QBLK)
                        lse_h = lse_r[rows, :]
                        dd_h = dd_r[rows, :]
                        sh = sc[r, :] + bias
                        p = jnp.exp(jnp.concatenate([sh[:, 0:LANES] - lse_h, sh[:, LANES:KWIN] - lse_h], axis=1))
                        dph = dp[r, :]
                        ds = p * jnp.concatenate([dph[:, 0:LANES] - dd_h, dph[:, LANES:KWIN] - dd_h], axis=1)
                        p_buf[slot, r, :] = p.astype(BF16)
                        ds_buf[slot, r, :] = ds.astype(BF16)

                def grads(b, slot):
                    base = _block_base(b)
                    rows = pl.ds(base, QBLK)
                    win = pl.ds(base, KWIN)
                    p = p_buf[slot]
                    ds = ds_buf[slot]
                    dq2 = jnp.dot(ds, k_rm[win, :], preferred_element_type=F32)
                    dq_rm[rows, :] = jnp.where(low, dq2[0:QBLK, :], dq2[QBLK:2 * QBLK, :])
                    dk_rm[win, :] += lax.dot_general(ds, stacked(q_rm, rows), tn_dims, preferred_element_type=F32)
                    dv_rm[win, :] += lax.dot_general(p, stacked(do_rm, rows), tn_dims, preferred_element_type=F32)

                _skewed_blocks(n_blk, scores, grads)
                _from_residue_major(dq_ref, dq_rm, s, d)
                _from_residue_major(dk_ref, dk_rm, s, d, src_off=HALF_SPAN)
                _from_residue_major(dv_ref, dv_rm, s, d, src_off=HALF_SPAN)

    qk_spec = pl.BlockSpec((s, LANES), lambda hp, g: (0, g * n_pairs + hp))
    v_spec = pl.BlockSpec((s, LANES), lambda hp, g: (0, v_col0 + g * n_pairs + hp))
    o_spec = pl.BlockSpec((s, LANES), lambda hp, g: (0, hp))
    width = qn.shape[1]
    f32buf = pltpu.VMEM((s, LANES), F32)
    f32pad = pltpu.VMEM((spad, LANES), F32)
    return _pallas(
        body, name=name, grid=(n_pairs, len(DILATIONS)),
        in_specs=[qk_spec, qk_spec, v_spec, o_spec, o_spec, o_spec,
                  pl.BlockSpec((LANES, LANES), lambda hp, g: (0, 0))],
        out_specs=[qk_spec, qk_spec, qk_spec],
        out_shape=[jax.ShapeDtypeStruct((s, width), F32)] * 3,
        operands=[qn, kn, proj, dattn, attn, lse, bd],
        scratch_shapes=[pltpu.VMEM((s, LANES), BF16), pltpu.VMEM((spad, LANES), BF16),
                        pltpu.VMEM((spad, LANES), BF16), pltpu.VMEM((s, LANES), BF16),
                        f32buf, f32buf, f32buf, f32buf, f32buf, f32pad, f32pad,
                        f32buf, f32buf, f32buf, f32buf, pltpu.VMEM((QBLK, KWIN), F32),
                        pltpu.VMEM((2, 2 * QBLK, KWIN), BF16), pltpu.VMEM((2, 2 * QBLK, KWIN), BF16)],
        rider=rider)


CONV_PAD = 16


def _conv_fwd(proj, conv_w, conv_b, col0, name, rider=None):
    s = proj.shape[0]
    ch = conv_w.shape[1]
    nblk = ch // LANES
    a0 = col0 // LANES
    tr = 256
    shift = CONV_PAD - (CONV_WIDTH - 1) // 2

    def body(a_ref, b_ref, w_ref, bias_ref, u0_ref, uc_ref, pad):
        z = jnp.zeros((CONV_PAD, LANES), F32)
        pad[0:CONV_PAD, :] = z
        pad[s + CONV_PAD: s + 2 * CONV_PAD, :] = z

        def glu(rows):
            u0 = a_ref[rows, :] * jax.nn.sigmoid(b_ref[rows, :])
            u0_ref[rows, :] = u0
            pad[pl.ds(rows.start + CONV_PAD, rows.size), :] = u0

        _row_chunks(s, glu)
        for t in range(0, s, tr):
            acc = jnp.broadcast_to(bias_ref[...], (tr, LANES))
            for k in range(CONV_WIDTH):
                acc = acc + w_ref[k:k + 1, :] * pad[t + k + shift: t + k + shift + tr, :]
            uc_ref[t:t + tr, :] = acc

    return _pallas(
        body, name=name, grid=(nblk,),
        in_specs=[pl.BlockSpec((s, LANES), lambda c: (0, a0 + c)),
                  pl.BlockSpec((s, LANES), lambda c: (0, a0 + nblk + c)),
                  pl.BlockSpec((CONV_WIDTH, LANES), lambda c: (0, c)),
                  pl.BlockSpec((1, LANES), lambda c: (0, c))],
        out_specs=[pl.BlockSpec((s, LANES), lambda c: (0, c))] * 2,
        out_shape=[jax.ShapeDtypeStruct((s, ch), F32)] * 2, operands=[proj, proj, conv_w, conv_b],
        scratch_shapes=[pltpu.VMEM((s + 2 * CONV_PAD, LANES), F32)], rider=rider)


def _ln_silu_fwd(uc, ln_w, ln_b, name):
    s, ch = uc.shape
    tm = 256

    def body(u_ref, w_ref, b_ref, o_ref):
        u = u_ref[...]
        mu = jnp.mean(u, axis=-1, keepdims=True)
        xc = u - mu
        rstd = lax.rsqrt(jnp.mean(xc * xc, axis=-1, keepdims=True) + EPS)
        z = xc * rstd * w_ref[...] + b_ref[...]
        o_ref[...] = (z * jax.nn.sigmoid(z)).astype(BF16)

    row = pl.BlockSpec((tm, ch), lambda i: (i, 0))
    vec = pl.BlockSpec((1, ch), lambda i: (0, 0))
    return pl.pallas_call(
        body, name=name, grid=(s // tm,), in_specs=[row, vec, vec], out_specs=row,
        out_shape=jax.ShapeDtypeStruct((s, ch), BF16), compiler_params=_params(),
    )(uc, ln_w, ln_b)


def _ln_silu_bwd(du3, uc, ln_w, ln_b, name):
    s, ch = uc.shape
    tm = 256

    def body(d_ref, u_ref, w_ref, b_ref, du_ref, dw_ref, db_ref):
        u = u_ref[...]
        mu = jnp.mean(u, axis=-1, keepdims=True)
        xc = u - mu
        rstd = lax.rsqrt(jnp.mean(xc * xc, axis=-1, keepdims=True) + EPS)
        xhat = xc * rstd
        z = xhat * w_ref[...] + b_ref[...]
        sg = jax.nn.sigmoid(z)
        dz = d_ref[...] * (sg * (1.0 + z * (1.0 - sg)))
        dxh = dz * w_ref[...]
        du_ref[...] = rstd * (dxh - jnp.mean(dxh, axis=-1, keepdims=True)
                              - xhat * jnp.mean(dxh * xhat, axis=-1, keepdims=True))
        pw = jnp.sum(dz * xhat, axis=0, keepdims=True)
        pb = jnp.sum(dz, axis=0, keepdims=True)
        first = pl.program_id(0) == 0

        @pl.when(first)
        def _():
            dw_ref[...] = pw
            db_ref[...] = pb

        @pl.when(jnp.logical_not(first))
        def _():
            dw_ref[...] += pw
            db_ref[...] += pb

    row = pl.BlockSpec((tm, ch), lambda i: (i, 0))
    vec = pl.BlockSpec((1, ch), lambda i: (0, 0))
    return pl.pallas_call(
        body, name=name, grid=(s // tm,), in_specs=[row, row, vec, vec], out_specs=[row, vec, vec],
        out_shape=[jax.ShapeDtypeStruct((s, ch), F32), jax.ShapeDtypeStruct((1, ch), F32),
                   jax.ShapeDtypeStruct((1, ch), F32)],
        compiler_params=_params(),
    )(du3, uc, ln_w, ln_b)


def _conv_bwd(duc, u0, proj, conv_w, col0, name, rider=None):
    s = proj.shape[0]
    ch = conv_w.shape[1]
    nblk = ch // LANES
    a0 = col0 // LANES
    tr = 256
    half = (CONV_WIDTH - 1) // 2
    shift = CONV_PAD - half

    def body(duc_ref, u0_ref, a_ref, b_ref, w_ref, da_ref, db_ref, dw_ref, dbias_ref, pad_d, pad_u):
        z = jnp.zeros((CONV_PAD, LANES), F32)
        for buf in (pad_d, pad_u):
            buf[0:CONV_PAD, :] = z
            buf[s + CONV_PAD: s + 2 * CONV_PAD, :] = z

        def fill(rows):
            dst = pl.ds(rows.start + CONV_PAD, rows.size)
            pad_d[dst, :] = duc_ref[rows, :]
            pad_u[dst, :] = u0_ref[rows, :]

        _row_chunks(s, fill)
        dw_acc = [jnp.zeros((8, LANES), F32) for _ in range(CONV_WIDTH)]
        dbias_acc = jnp.zeros((8, LANES), F32)
        for t in range(0, s, tr):
            d_t = duc_ref[t:t + tr, :]
            dbias_acc = dbias_acc + jnp.sum(d_t.reshape(tr // 8, 8, LANES), axis=0)
            du0 = jnp.zeros((tr, LANES), F32)
            for k in range(CONV_WIDTH):
                du0 = du0 + w_ref[k:k + 1, :] * pad_d[t - k + half + CONV_PAD: t - k + half + CONV_PAD + tr, :]
                prod = d_t * pad_u[t + k + shift: t + k + shift + tr, :]
                dw_acc[k] = dw_acc[k] + jnp.sum(prod.reshape(tr // 8, 8, LANES), axis=0)
            av = a_ref[t:t + tr, :]
            sg = jax.nn.sigmoid(b_ref[t:t + tr, :])
            da_ref[t:t + tr, :] = (du0 * sg).astype(BF16)
            db_ref[t:t + tr, :] = (du0 * av * sg * (1.0 - sg)).astype(BF16)
        for k in range(CONV_WIDTH):
            dw_ref[k:k + 1, :] = jnp.sum(dw_acc[k], axis=0, keepdims=True)
        dbias_ref[...] = jnp.sum(dbias_acc, axis=0, keepdims=True)

    col = lambda off: pl.BlockSpec((s, LANES), lambda c: (0, off + c))
    return _pallas(
        body, name=name, grid=(nblk,),
        in_specs=[col(0), col(0), col(a0), col(a0 + nblk),
                  pl.BlockSpec((CONV_WIDTH, LANES), lambda c: (0, c))],
        out_specs=[col(0), col(0), pl.BlockSpec((CONV_WIDTH, LANES), lambda c: (0, c)),
                   pl.BlockSpec((1, LANES), lambda c: (0, c))],
        out_shape=[jax.ShapeDtypeStruct((s, ch), BF16)] * 2
        + [jax.ShapeDtypeStruct((CONV_WIDTH, ch), F32), jax.ShapeDtypeStruct((1, ch), F32)],
        operands=[duc, u0, proj, proj, conv_w],
        scratch_shapes=[pltpu.VMEM((s + 2 * CONV_PAD, LANES), F32)] * 2, rider=rider)


GATE_BLK = 512


def _gate_fwd(proj, bg, y_a, y_b, col0, name):
    s, d = y_a.shape
    tm = 256
    g0 = col0 // GATE_BLK
    nb = d // GATE_BLK

    def body(ga_ref, gb_ref, ba_ref, bb_ref, ya_ref, yb_ref, o_ref):
        g_a = jax.nn.sigmoid(ga_ref[...] + ba_ref[...])
        g_b = jax.nn.sigmoid(gb_ref[...] + bb_ref[...])
        o_ref[...] = (g_a * ya_ref[...] + g_b * yb_ref[...]).astype(BF16)

    act = pl.BlockSpec((tm, GATE_BLK), lambda i, j: (i, j))
    return pl.pallas_call(
        body, name=name, grid=(s // tm, nb),
        in_specs=[pl.BlockSpec((tm, GATE_BLK), lambda i, j: (i, g0 + j)),
                  pl.BlockSpec((tm, GATE_BLK), lambda i, j: (i, g0 + nb + j)),
                  pl.BlockSpec((None, 1, GATE_BLK), lambda i, j: (0, 0, j)),
                  pl.BlockSpec((None, 1, GATE_BLK), lambda i, j: (1, 0, j)), act, act],
        out_specs=act, out_shape=jax.ShapeDtypeStruct((s, d), BF16), compiler_params=_params(),
    )(proj, proj, bg, bg, y_a, y_b)


def _gate_bwd(d_mixed, proj, bg, y_a, y_b, col0, name, rider=None):
    s, d = y_a.shape
    tm = 256
    half = d // 2
    assert col0 % half == 0
    c0 = col0 // half

    def body(dm_ref, a0_ref, a1_ref, b0_ref, b1_ref, bias_ref, ya_ref, yb_ref, dgl_ref, dya_ref, dyb_ref, db_ref):
        dm = dm_ref[...]
        parts = []
        for br, (lo_ref, hi_ref, y_ref, dy_ref) in enumerate(((a0_ref, a1_ref, ya_ref, dya_ref),
                                                              (b0_ref, b1_ref, yb_ref, dyb_ref))):
            logits = jnp.concatenate([lo_ref[...], hi_ref[...]], axis=1)
            gate = jax.nn.sigmoid(logits + bias_ref[br])
            dy_ref[...] = (dm * gate).astype(BF16)
            dgl = dm * y_ref[...] * gate * (1.0 - gate)
            dgl_ref[:, br * d:(br + 1) * d] = dgl.astype(BF16)
            parts.append(jnp.sum(dgl, axis=0, keepdims=True))
        part = jnp.concatenate(parts, axis=0)
        first = pl.program_id(0) == 0

        @pl.when(first)
        def _():
            db_ref[...] = part

        @pl.when(jnp.logical_not(first))
        def _():
            db_ref[...] += part

    row = pl.BlockSpec((tm, d), lambda i: (i, 0))
    logit_blk = lambda k: pl.BlockSpec((tm, half), functools.partial(lambda i, k: (i, c0 + k), k=k))
    return _pallas(
        body, name=name, grid=(s // tm,),
        in_specs=[row, logit_blk(0), logit_blk(1), logit_blk(2), logit_blk(3),
                  pl.BlockSpec((2, 1, d), lambda i: (0, 0, 0)), row, row],
        out_specs=[pl.BlockSpec((tm, 2 * d), lambda i: (i, 0)), row, row, pl.BlockSpec((2, d), lambda i: (0, 0))],
        out_shape=[jax.ShapeDtypeStruct((s, 2 * d), BF16), jax.ShapeDtypeStruct((s, d), BF16),
                   jax.ShapeDtypeStruct((s, d), BF16), jax.ShapeDtypeStruct((2, d), F32)],
        operands=[d_mixed, proj, proj, proj, proj, bg, y_a, y_b], rider=rider)


def _ffn_in_swiglu(h2, w_blocked, name):
    s, k = h2.shape
    nblk, _, tn = w_blocked.shape
    ff = nblk // 2 * tn
    tm = 512

    def body(a_ref, wg_ref, wu_ref, g_ref, u_ref, act_ref):
        a = a_ref[...]
        gt = jnp.dot(a, wg_ref[...], preferred_element_type=F32)
        up = jnp.dot(a, wu_ref[...], preferred_element_type=F32)
        g_ref[...] = gt
        u_ref[...] = up
        act_ref[...] = (gt * jax.nn.sigmoid(gt) * up).astype(BF16)

    out = pl.BlockSpec((tm, tn), lambda j, i: (i, j))
    return pl.pallas_call(
        body, name=name, grid=(nblk // 2, s // tm),
        in_specs=[pl.BlockSpec((tm, k), lambda j, i: (i, 0)),
                  pl.BlockSpec((None, k, tn), lambda j, i: (j, 0, 0)),
                  pl.BlockSpec((None, k, tn), lambda j, i: (nblk // 2 + j, 0, 0))],
        out_specs=[out, out, out],
        out_shape=[jax.ShapeDtypeStruct((s, ff), F32), jax.ShapeDtypeStruct((s, ff), F32),
                   jax.ShapeDtypeStruct((s, ff), BF16)],
        compiler_params=_params(),
    )(h2, w_blocked, w_blocked)


def _swiglu_bwd(gate, up, d_act, name, rider=None):
    s, ff = gate.shape
    tm = 256

    def body(g_ref, u_ref, d_ref, o_ref):
        gt = g_ref[...]
        sg = jax.nn.sigmoid(gt)
        dv = d_ref[...]
        o_ref[:, 0:ff] = (dv * u_ref[...] * (sg * (1.0 + gt * (1.0 - sg)))).astype(BF16)
        o_ref[:, ff:2 * ff] = (dv * gt * sg).astype(BF16)

    row = pl.BlockSpec((tm, ff), lambda i: (i, 0))
    return _pallas(
        body, name=name, grid=(s // tm,), in_specs=[row, row, row],
        out_specs=pl.BlockSpec((tm, 2 * ff), lambda i: (i, 0)),
        out_shape=jax.ShapeDtypeStruct((s, 2 * ff), BF16), operands=[gate, up, d_act], rider=rider)


def _out_proj_rmsnorm(mixed, w_out, x, norm_w, name):
    s, k = mixed.shape
    d = w_out.shape[1]
    tm = 512

    def body(a_ref, w_ref, x_ref, nw_ref, x1_ref, h2_ref):
        x1 = x_ref[...] + jnp.dot(a_ref[...], w_ref[...], preferred_element_type=F32)
        x1_ref[...] = x1
        rstd = lax.rsqrt(jnp.mean(x1 * x1, axis=-1, keepdims=True) + EPS)
        h2_ref[...] = (x1 * rstd * nw_ref[...]).astype(BF16)

    row = pl.BlockSpec((tm, d), lambda i: (i, 0))
    return pl.pallas_call(
        body, name=name, grid=(s // tm,),
        in_specs=[pl.BlockSpec((tm, k), lambda i: (i, 0)), pl.BlockSpec((k, d), lambda i: (0, 0)), row,
                  pl.BlockSpec((1, d), lambda i: (0, 0))],
        out_specs=[row, row],
        out_shape=[jax.ShapeDtypeStruct((s, d), F32), jax.ShapeDtypeStruct((s, d), BF16)],
        compiler_params=_params(),
    )(mixed, w_out, x, norm_w)


def _ffn_out_loss(act, w_ffn_out, x1, target, name):
    s, k = act.shape
    d = w_ffn_out.shape[1]
    tm = 512

    def body(a_ref, w_ref, x1_ref, t_ref, dy_ref, dyb_ref, loss_ref, acc):
        y = x1_ref[...] + jnp.dot(a_ref[...], w_ref[...], preferred_element_type=F32)
        diff = y - t_ref[...]
        dy = diff * (1.0 / d)
        dy_ref[...] = dy
        dyb_ref[...] = dy.astype(BF16)
        part = jnp.sum((diff * diff).reshape(tm // 8, 8, d), axis=0)
        i = pl.program_id(0)

        @pl.when(i == 0)
        def _():
            acc[...] = part

        @pl.when(i > 0)
        def _():
            acc[...] += part

        @pl.when(i == pl.num_programs(0) - 1)
        def _():
            loss_ref[...] = (0.5 / d) * jnp.sum(jnp.sum(acc[...], axis=1, keepdims=True), axis=0, keepdims=True)

    row = pl.BlockSpec((tm, d), lambda i: (i, 0))
    return pl.pallas_call(
        body, name=name, grid=(s // tm,),
        in_specs=[pl.BlockSpec((tm, k), lambda i: (i, 0)), pl.BlockSpec((k, d), lambda i: (0, 0)), row, row],
        out_specs=[row, row, pl.BlockSpec((1, 1), lambda i: (0, 0))],
        out_shape=[jax.ShapeDtypeStruct((s, d), F32), jax.ShapeDtypeStruct((s, d), BF16),
                   jax.ShapeDtypeStruct((1, 1), F32)],
        scratch_shapes=[pltpu.VMEM((8, d), F32)], compiler_params=_params(),
    )(act, w_ffn_out, x1, target)


LATE_GATHER = ("w_o_attn", "w_pw_conv", "w_out", "w_ffn_in", "w_ffn_out")
EARLY_REDUCE = LATE_GATHER


def _blocks_by_half(g):
    if g.ndim == 2:
        g = g.reshape(N_CHIPS, g.shape[0] // N_CHIPS, g.shape[1])
    return g.reshape(N_CHIPS, 2, g.shape[1] // 2, g.shape[2])


def _forward_backward(x, pos_col, target, wts, late_bufs, pos_arr):
    wts = dict(wts)
    consts = _rope_consts()
    bd = consts[3]
    qw2 = jnp.tile(wts["q_norm_w"], (1, LANES // HEAD_DIM))
    kw2 = jnp.tile(wts["k_norm_w"], (1, LANES // HEAD_DIM))
    qkv_w = 3 * N_SLOT_HEADS * HEAD_DIM
    conv_col0 = 3 * qkv_w
    ch = wts["conv_w"].shape[1]
    gate_col0 = conv_col0 + 2 * ch

    h = _rmsnorm_fwd(x, wts["norm1_w"], "rms1_fwd")
    gathering, started = _split_start(_gather_ici_rider(late_bufs, []), "late_gather_start", after=[wts["w_in"]])
    proj = _matmul(h, wts["w_in"], mode="nn", tm=512, tn=1920, tk=1024, out_dtype=F32, name="mm_proj",
                   b_blocked=True, cols_outer=True, after=[started])
    qn, kn = _qk_fwd(proj, pos_col, qw2, kw2, consts, "qk_fwd")
    attn, lse, attn_b = _attn_fwd(qn, kn, proj, "attn_fwd")
    late_bufs, _ = _split_wait(gathering, after=[attn_b], name="late_gather_wait")
    (u0, uc), late_bufs = _conv_fwd(proj, wts["conv_w"], wts["conv_b"], conv_col0, "conv_fwd",
                                    rider=_gather_forward_rider(late_bufs))
    for n, buf in zip(LATE_GATHER, late_bufs):
        full = buf.reshape(N_CHIPS, -1, buf.shape[3])
        wts[n] = full.reshape(-1, full.shape[2]) if n in ROW_SHARDED else full
    y_a = _matmul(attn_b, wts["w_o_attn"], mode="nn", tm=1024, tn=256, tk=512, out_dtype=F32, name="mm_ya",
                  b_blocked=True)
    u3 = _ln_silu_fwd(uc, wts["conv_ln_w"], wts["conv_ln_b"], "ln_fwd")
    y_b = _matmul(u3, wts["w_pw_conv"], mode="nn", tm=1024, tn=256, tk=512, out_dtype=F32, name="mm_yb",
                  b_blocked=True)
    mixed = _gate_fwd(proj, wts["b_gate"], y_a, y_b, gate_col0, "gate_fwd")
    x1, h2 = _out_proj_rmsnorm(mixed, wts["w_out"], x, wts["norm2_w"], "mm_x1_rms2")
    gate, up, act = _ffn_in_swiglu(h2, wts["w_ffn_in"], "mm_gu_swiglu")
    dy, dy_b16, loss = _ffn_out_loss(act, wts["w_ffn_out"], x1, target, "mm_x2_loss")

    g = {}
    by_chip = {}

    def pair_add(n, blocks, received):
        return _add_own_half(blocks, received, pos_arr, f"grads_pair_add_{n}")

    d_act = _matmul(dy_b16, wts["w_ffn_out"], mode="nt", tm=512, tn=1408, tk=1024, out_dtype=F32, name="mm_dact",
                    cols_outer=True)
    g_ffn_out = _blocks_by_half(
        _matmul(act, dy_b16, mode="tn", tm=1408, tn=1024, tk=2048, out_dtype=F32, name="mm_dwffnout"))
    dgu, (received,) = _swiglu_bwd(gate, up, d_act, "swiglu_bwd",
                                   rider=_pair_exchange_rider([g_ffn_out], halved=True))
    to_send, own = pair_add("w_ffn_out", g_ffn_out, received)
    dh2, (by_chip["w_ffn_out"],) = _matmul(
        dgu, wts["w_ffn_in"], mode="nt", tm=1024, tn=1024, tk=1408, out_dtype=F32, name="mm_dh2", b_blocked=True,
        rider=_chip_exchange_rider([to_send], [own]))
    g_ffn_in = _blocks_by_half(_matmul(h2, dgu, mode="tn", tm=512, tn=1408, tk=2048, out_dtype=F32,
                                       name="mm_dwffnin", out_blocked=N_CHIPS, cols_outer=True))
    exchanging, started = _split_start(_pair_exchange_rider([g_ffn_in], halved=True), "grads_ffn_in_pair_start")
    dx1, dx1_b16, g["norm2_w"] = _rmsnorm_bwd(dh2, x1, wts["norm2_w"], dy, "rms2_bwd")
    d_mixed = _matmul(dx1_b16, wts["w_out"], mode="nt", tm=512, tn=1024, tk=1024, out_dtype=F32, name="mm_dmixed",
                      after=[started])
    g["w_out"] = _matmul(mixed, dx1_b16, mode="tn", tm=512, tn=1024, tk=2048, out_dtype=F32, name="mm_dwout")
    dgl, dy_a, dy_b, g["b_gate"] = _gate_bwd(d_mixed, proj, wts["b_gate"], y_a, y_b, gate_col0, "gate_bwd")
    (received,), (g_ffn_in,) = _split_wait(exchanging, after=[dgl], name="grads_ffn_in_pair_wait")
    ffn_in_to_send, ffn_in_own = pair_add("w_ffn_in", g_ffn_in, received)
    dattn = _matmul(dy_a, wts["w_o_attn"], mode="nt", tm=1024, tn=512, tk=256, out_dtype=F32, name="mm_dattn",
                    b_blocked=True)
    g["w_o_attn"] = _matmul(attn_b, dy_a, mode="tn", tm=512, tn=256, tk=2048, out_dtype=F32, name="mm_dwo",
                            out_blocked=N_CHIPS)
    du3 = _matmul(dy_b, wts["w_pw_conv"], mode="nt", tm=1024, tn=512, tk=256, out_dtype=F32, name="mm_du3",
                  b_blocked=True)
    g["w_pw_conv"] = _matmul(u3, dy_b, mode="tn", tm=512, tn=256, tk=2048, out_dtype=F32, name="mm_dwpw",
                             out_blocked=N_CHIPS)
    duc, g["conv_ln_w"], g["conv_ln_b"] = _ln_silu_bwd(du3, uc, wts["conv_ln_w"], wts["conv_ln_b"], "ln_bwd")

    small3 = ("w_out", "w_o_attn", "w_pw_conv")
    g_small3 = [_blocks_by_half(g.pop(n)) for n in small3]
    (da, db, g["conv_w"], g["conv_b"]), received = _conv_bwd(
        duc, u0, proj, wts["conv_w"], conv_col0, "conv_bwd", rider=_pair_exchange_rider(g_small3, halved=True))
    sums3 = [pair_add(n, gb, rv) for n, gb, rv in zip(small3, g_small3, received)]
    (dqn, dkn, dv), (by_chip["w_ffn_in"],) = _attn_bwd(
        qn, kn, proj, dattn, attn, lse, bd, "attn_bwd",
        rider=_chip_exchange_rider([ffn_in_to_send], [ffn_in_own]))
    (dproj, dqw, dkw), exchanged3 = _qk_bwd(
        dqn, dkn, dv, da, db, dgl, proj, pos_col, qw2, kw2, consts, "qk_bwd",
        rider=_chip_exchange_rider([s[0] for s in sums3], [s[1] for s in sums3]))
    by_chip.update(zip(small3, exchanged3))
    halves = [_sum_chips(by_chip[n], pos_arr, f"grads_chip_sum_{n}") for n in EARLY_REDUCE]
    g["q_norm_w"] = dqw[:, :HEAD_DIM]
    g["k_norm_w"] = dkw[:, :HEAD_DIM]

    c = pos_arr[0]
    rh = h.shape[1] // 2
    h_sibling = lax.dynamic_slice_in_dim(h, (1 - c) * rh, rh, axis=1)
    h_own = lax.dynamic_slice_in_dim(h, c * rh, rh, axis=1)
    g_sibling, shards = _matmul(h_sibling, dproj, mode="tn", tm=rh, tn=1920, tk=2048, out_dtype=F32,
                                name="mm_dwin_sibling", out_blocked=N_CHIPS, rider=_pair_gather_rider(halves))
    reduced = dict(zip(EARLY_REDUCE, shards))
    exchanging, started = _split_start(_pair_exchange_rider([g_sibling], halved=False), "grads_w_in_pair_start")
    g_own = _matmul(h_own, dproj, mode="tn", tm=rh, tn=1920, tk=2048, out_dtype=F32, name="mm_dwin_own",
                    out_blocked=N_CHIPS, after=[started])
    (from_sibling,), _ = _split_wait(exchanging, after=[g_own], name="grads_w_in_pair_wait")
    to_send, own = _add_own_half(g_own, from_sibling, pos_arr, "grads_pair_add_w_in")
    in_flight, started = _split_start(_chip_exchange_rider([to_send], [own]), "grads_w_in_exchange_start")
    dh = _matmul(dproj, wts["w_in"], mode="nt", tm=1024, tn=1024, tk=1920, out_dtype=F32, name="mm_dh",
                 b_blocked=True, after=[started])
    grad_x, _, g["norm1_w"] = _rmsnorm_bwd(dh, x, wts["norm1_w"], dx1, "rms1_bwd")
    return loss, grad_x, g, reduced, (in_flight, started)


def _mesh_pos():
    return lax.axis_index("x"), lax.axis_index("y"), lax.axis_index("c")


def _other_chips(x, y):
    return [(1 - x, y), (x, 1 - y), (1 - x, 1 - y)]


def _cast_into_slot(shard, chip_arr, dtype, name):
    r, c = shard.shape
    tr = r // 2 if r % 32 == 0 else r

    def body(chip_ref, s_ref, o_ref):
        del chip_ref
        o_ref[...] = s_ref[...].astype(dtype)

    return pl.pallas_call(
        body, name=name,
        grid_spec=pltpu.PrefetchScalarGridSpec(
            num_scalar_prefetch=1, grid=(r // tr,),
            in_specs=[pl.BlockSpec((tr, c), lambda i, chip_ref: (i, 0))],
            out_specs=pl.BlockSpec((None, tr, c), lambda i, chip_ref: (chip_ref[0], i, 0))),
        out_shape=jax.ShapeDtypeStruct((N_CHIPS, r, c), dtype), compiler_params=_params(),
    )(chip_arr, shard)


GATHER_CHUNKS = 4


def _gather_both_legs_rider(big, small):
    nb = len(big)
    n = nb + len(small)
    nch = GATHER_CHUNKS

    def part(bufs, a, slot, half, ch):
        if a >= nb:
            return bufs[a].at[slot]
        rows = bufs[a].shape[2] // nch
        return bufs[a].at[slot, half, pl.ds(ch * rows, rows)]

    def pieces():
        return [(a, ch, k) for ch in range(nch) for a in range(n) for k in range(3) if a < nb or ch == 0]

    def ici(bufs, sems, a, ch, k, slot_of_src):
        x, y, c = _mesh_pos()
        px, py = _other_chips(x, y)[k]
        slot = 2 * x + y if slot_of_src == "mine" else 2 * px + py
        return pltpu.make_async_remote_copy(
            src_ref=part(bufs, a, slot, c, ch), dst_ref=part(bufs, a, slot, c, ch), send_sem=sems[0].at[a, ch, k],
            recv_sem=sems[1].at[a, ch, k], device_id=(px, py, c), device_id_type=MESH)

    def forward(bufs, sems, a, ch, k, half):
        x, y, c = _mesh_pos()
        px, py = _other_chips(x, y)[k]
        h = c if half == "mine" else 1 - c
        return pltpu.make_async_remote_copy(
            src_ref=part(bufs, a, 2 * px + py, h, ch), dst_ref=part(bufs, a, 2 * px + py, h, ch),
            send_sem=sems[2].at[a, ch, k], recv_sem=sems[3].at[a, ch, k], device_id=(x, y, 1 - c),
            device_id_type=MESH)

    def start(r_in, bufs, sems):
        for a, ch, k in pieces():
            ici(bufs, sems, a, ch, k, "mine").start()

    def wait(r_in, bufs, sems):
        for a, ch, k in pieces():
            ici(bufs, sems, a, ch, k, "theirs").wait_recv()
            if a < nb:
                forward(bufs, sems, a, ch, k, "mine").start()
        for a, ch, k in pieces():
            if a < nb:
                forward(bufs, sems, a, ch, k, "theirs").wait_recv()
        for a, ch, k in pieces():
            ici(bufs, sems, a, ch, k, "mine").wait_send()
            if a < nb:
                forward(bufs, sems, a, ch, k, "mine").wait_send()

    ops = list(big) + list(small)
    return _Rider(ops, [jax.ShapeDtypeStruct(o.shape, o.dtype) for o in ops], {i: i for i in range(n)},
                  [pltpu.SemaphoreType.DMA((n, nch, 3)), pltpu.SemaphoreType.DMA((n, nch, 3)),
                   pltpu.SemaphoreType.DMA((nb, nch, 3)), pltpu.SemaphoreType.DMA((nb, nch, 3))], start, wait)


def _comm_call(rider, name):
    def body():
        pass

    return _pallas(body, name=name, grid=(1,), in_specs=[], out_specs=[], out_shape=[], operands=[],
                   rider=rider)[1]


def _gather_ici_rider(big, small):
    nb = len(big)
    n = nb + len(small)

    def copies(bufs, sems):
        x, y, c = _mesh_pos()
        me = 2 * x + y
        part = lambda a, slot: bufs[a].at[slot, c] if a < nb else bufs[a].at[slot]
        out = []
        for a in range(n):
            for k, (px, py) in enumerate(_other_chips(x, y)):
                send = functools.partial(
                    pltpu.make_async_remote_copy,
                    src_ref=part(a, me), dst_ref=part(a, me), send_sem=sems[0].at[a, k],
                    recv_sem=sems[1].at[a, k], device_id=(px, py, c), device_id_type=MESH)
                recv = functools.partial(
                    pltpu.make_async_remote_copy,
                    src_ref=part(a, 2 * px + py), dst_ref=part(a, 2 * px + py), send_sem=sems[0].at[a, k],
                    recv_sem=sems[1].at[a, k], device_id=(px, py, c), device_id_type=MESH)
                out.append((send, recv))
        return out

    def start(r_in, r_out, sems):
        for send, _ in copies(r_out, sems):
            send().start()

    def wait(r_in, r_out, sems):
        cps = copies(r_out, sems)
        for _, recv in cps:
            recv().wait_recv()
        for send, _ in cps:
            send().wait_send()

    ops = list(big) + list(small)
    return _Rider(ops, [jax.ShapeDtypeStruct(o.shape, o.dtype) for o in ops], {i: i for i in range(n)},
                  [pltpu.SemaphoreType.DMA((n, 3)), pltpu.SemaphoreType.DMA((n, 3))], start, wait)


def _gather_forward_rider(big):
    n = len(big)

    def copies(bufs, sems):
        x, y, c = _mesh_pos()
        out = []
        for a in range(n):
            for k, (px, py) in enumerate(_other_chips(x, y)):
                slot = 2 * px + py
                send = functools.partial(
                    pltpu.make_async_remote_copy,
                    src_ref=bufs[a].at[slot, c], dst_ref=bufs[a].at[slot, c], send_sem=sems[0].at[a, k],
                    recv_sem=sems[1].at[a, k], device_id=(x, y, 1 - c), device_id_type=MESH)
                recv = functools.partial(
                    pltpu.make_async_remote_copy,
                    src_ref=bufs[a].at[slot, 1 - c], dst_ref=bufs[a].at[slot, 1 - c], send_sem=sems[0].at[a, k],
                    recv_sem=sems[1].at[a, k], device_id=(x, y, 1 - c), device_id_type=MESH)
                out.append((send, recv))
        return out

    def start(r_in, r_out, sems):
        for send, _ in copies(r_out, sems):
            send().start()

    def wait(r_in, r_out, sems):
        cps = copies(r_out, sems)
        for _, recv in cps:
            recv().wait_recv()
        for send, _ in cps:
            send().wait_send()

    return _Rider(big, [jax.ShapeDtypeStruct(o.shape, o.dtype) for o in big], {i: i for i in range(n)},
                  [pltpu.SemaphoreType.DMA((n, 3)), pltpu.SemaphoreType.DMA((n, 3))], start, wait)


def _pair_exchange_rider(gs, halved):
    n = len(gs)

    def copies(r_in, r_out, sems):
        x, y, c = _mesh_pos()
        return [pltpu.make_async_remote_copy(
            src_ref=r_in[a].at[:, 1 - c] if halved else r_in[a], dst_ref=r_out[a], send_sem=sems[0].at[a],
            recv_sem=sems[1].at[a], device_id=(x, y, 1 - c), device_id_type=MESH) for a in range(n)]

    def start(r_in, r_out, sems):
        for cp in copies(r_in, r_out, sems):
            cp.start()

    def wait(r_in, r_out, sems):
        for cp in copies(r_in, r_out, sems):
            cp.wait()

    return _Rider(gs, [jax.ShapeDtypeStruct((g.shape[0],) + g.shape[-2:], g.dtype) for g in gs], {},
                  [pltpu.SemaphoreType.DMA((n,)), pltpu.SemaphoreType.DMA((n,))], start, wait)


def _chip_exchange_rider(to_send, by_chip, row_range=None):
    n = len(to_send)

    def copies(r_in, r_out, sems):
        x, y, c = _mesh_pos()
        me = 2 * x + y
        rows = (lambda ref: ref) if row_range is None else (lambda ref: ref.at[pl.ds(*row_range)])
        out = []
        for a in range(n):
            for k, (px, py) in enumerate(_other_chips(x, y)):
                send = functools.partial(
                    pltpu.make_async_remote_copy,
                    src_ref=rows(r_in[a].at[2 * px + py]), dst_ref=rows(r_out[a].at[me]),
                    send_sem=sems[0].at[a, k], recv_sem=sems[1].at[a, k], device_id=(px, py, c),
                    device_id_type=MESH)
                recv = functools.partial(
                    pltpu.make_async_remote_copy,
                    src_ref=rows(r_in[a].at[me]), dst_ref=rows(r_out[a].at[2 * px + py]),
                    send_sem=sems[0].at[a, k], recv_sem=sems[1].at[a, k], device_id=(px, py, c),
                    device_id_type=MESH)
                out.append((send, recv))
        return out

    def start(r_in, r_out, sems):
        for send, _ in copies(r_in, r_out, sems):
            send().start()

    def wait(r_in, r_out, sems):
        cps = copies(r_in, r_out, sems)
        for _, recv in cps:
            recv().wait_recv()
        for send, _ in cps:
            send().wait_send()

    return _Rider(list(to_send) + list(by_chip), [jax.ShapeDtypeStruct(b.shape, b.dtype) for b in by_chip],
                  {n + i: i for i in range(n)},
                  [pltpu.SemaphoreType.DMA((n, 3)), pltpu.SemaphoreType.DMA((n, 3))], start, wait)


HBM = pl.BlockSpec(memory_space=pltpu.HBM)
SEM = pl.BlockSpec(memory_space=pltpu.SEMAPHORE)


_IN_FLIGHT = pltpu.CompilerParams(has_side_effects=pltpu.SideEffectType.DATAFLOW_SIDE_EFFECTING)


class _FlatSems:
    def __init__(self, ref, shape):
        self.ref, self.shape = ref, shape

    @property
    def at(self):
        return self

    def __getitem__(self, idx):
        idx = idx if isinstance(idx, tuple) else (idx,)
        flat = 0
        for i, n in zip(idx, self.shape):
            flat = flat * n + i
        return self.ref.at[flat]


def _flat_sem_types(rider):
    return tuple(pltpu.SemaphoreType.DMA((int(np.prod(s.shape)),)) for s in rider.scratch)


def _as_rider_sems(rider, refs):
    return [_FlatSems(r, s.shape) for r, s in zip(refs, rider.scratch)]


def _split_start(rider, name, after=()):
    n_in, n_out, n_sem = len(rider.operands), len(rider.out_shapes), len(rider.scratch)
    n_after = len(after)
    fresh = [j for j in range(n_out) if j not in rider.aliases.values()]
    by_out = {j: i for i, j in rider.aliases.items()}

    def body(*refs):
        r_in = refs[:n_in]
        refs = refs[n_in + n_after:]
        sems = refs[:n_sem]
        thru = refs[n_sem:n_sem + n_in]
        fresh_refs = refs[n_sem + n_in:n_sem + n_in + len(fresh)]
        token = refs[-1]
        r_out = [thru[by_out[j]] if j in by_out else fresh_refs[fresh.index(j)] for j in range(n_out)]
        rider.start(r_in, r_out, _as_rider_sems(rider, sems))
        token[...] = jnp.zeros_like(token)

    res = pl.pallas_call(
        body, name=name,
        out_shape=_flat_sem_types(rider) + tuple(pltpu.HBM(o.shape, o.dtype) for o in rider.operands)
        + tuple(pltpu.HBM(rider.out_shapes[j].shape, rider.out_shapes[j].dtype) for j in fresh)
        + (jax.ShapeDtypeStruct((8, LANES), F32),),
        in_specs=(HBM,) * n_in + (ANY,) * n_after,
        out_specs=(SEM,) * n_sem + (HBM,) * (n_in + len(fresh)) + (pl.BlockSpec(memory_space=pltpu.VMEM),),
        input_output_aliases={i: n_sem + i for i in range(n_in)}, compiler_params=_IN_FLIGHT,
    )(*[pltpu.with_memory_space_constraint(o, pltpu.HBM) for o in rider.operands], *after)
    return (rider, res[:n_sem], res[n_sem:n_sem + n_in], res[n_sem + n_in:-1]), res[-1]


def _split_wait(handles, after, name):
    rider, sems, thru, fresh_arrays = handles
    n_in, n_out, n_sem = len(rider.operands), len(rider.out_shapes), len(rider.scratch)
    fresh = [j for j in range(n_out) if j not in rider.aliases.values()]
    by_out = {j: i for i, j in rider.aliases.items()}
    n_data = n_in + len(fresh)

    def body(*refs):
        r_in = refs[:n_in]
        fresh_refs = refs[n_in:n_data]
        sem_refs = refs[n_data:n_data + n_sem]
        r_out = [r_in[by_out[j]] if j in by_out else fresh_refs[fresh.index(j)] for j in range(n_out)]
        rider.wait(r_in, r_out, _as_rider_sems(rider, sem_refs))

    data = list(thru) + list(fresh_arrays)
    res = pl.pallas_call(
        body, name=name, out_shape=tuple(pltpu.HBM(d.shape, d.dtype) for d in data),
        in_specs=(HBM,) * n_data + (SEM,) * n_sem + (ANY,) * len(after), out_specs=(HBM,) * n_data,
        input_output_aliases={i: i for i in range(n_data)}, compiler_params=_IN_FLIGHT,
    )(*data, *sems, *after)
    return [res[by_out[j]] if j in by_out else res[n_in + fresh.index(j)] for j in range(n_out)], res[:n_in]


def _pair_gather_rider(bufs):
    n = len(bufs)

    def copies(r_out, sems):
        x, y, c = _mesh_pos()
        out = []
        for a in range(n):
            send = functools.partial(
                    pltpu.make_async_remote_copy,
                src_ref=r_out[a].at[c], dst_ref=r_out[a].at[c], send_sem=sems[0].at[a],
                recv_sem=sems[1].at[a], device_id=(x, y, 1 - c), device_id_type=MESH)
            recv = functools.partial(
                    pltpu.make_async_remote_copy,
                src_ref=r_out[a].at[1 - c], dst_ref=r_out[a].at[1 - c], send_sem=sems[0].at[a],
                recv_sem=sems[1].at[a], device_id=(x, y, 1 - c), device_id_type=MESH)
            out.append((send, recv))
        return out

    def start(r_in, r_out, sems):
        for send, _ in copies(r_out, sems):
            send().start()

    def wait(r_in, r_out, sems):
        cps = copies(r_out, sems)
        for _, recv in cps:
            recv().wait_recv()
        for send, _ in cps:
            send().wait_send()

    return _Rider(bufs, [jax.ShapeDtypeStruct(b.shape, b.dtype) for b in bufs], {i: i for i in range(n)},
                  [pltpu.SemaphoreType.DMA((n,)), pltpu.SemaphoreType.DMA((n,))], start, wait)


def _add_own_half(g, recv, pos_arr, name):
    nb, rh, cols = g.shape[0], g.shape[-2], g.shape[-1]

    def body(pos_ref, g_ref, r_ref, send_ref, own_ref):
        s = (g_ref[...] + r_ref[...]).astype(BF16)
        send_ref[...] = s

        @pl.when(pl.program_id(0) == pos_ref[1])
        def _():
            own_ref[...] = s

    blk = pl.BlockSpec((None, rh, cols), lambda j, pos_ref: (j, 0, 0))
    g_spec = blk if g.ndim == 3 else pl.BlockSpec((None, None, rh, cols),
                                                   lambda j, pos_ref: (j, pos_ref[0], 0, 0))
    shape = jax.ShapeDtypeStruct((nb, rh, cols), BF16)
    return pl.pallas_call(
        body, name=name,
        grid_spec=pltpu.PrefetchScalarGridSpec(
            num_scalar_prefetch=1, grid=(nb,), in_specs=[g_spec, blk],
            out_specs=[blk, pl.BlockSpec((None, rh, cols), lambda j, pos_ref: (pos_ref[1], 0, 0))]),
        out_shape=[shape, shape], compiler_params=_params(),
    )(pos_arr, g, recv)


def _sum_chips(gath, pos_arr, name):
    nb, rh, cols = gath.shape

    def body(pos_ref, a_ref, b_ref, c_ref, d_ref, o_ref):
        del pos_ref
        o_ref[...] = ((a_ref[...].astype(F32) + b_ref[...].astype(F32)) + c_ref[...].astype(F32)) \
            + d_ref[...].astype(F32)

    tr = rh // 2 if (rh // 2) % 16 == 0 else rh
    specs = [pl.BlockSpec((None, tr, cols), functools.partial(lambda i, pos_ref, j: (j, i, 0), j=j))
             for j in range(nb)]
    return pl.pallas_call(
        body, name=name,
        grid_spec=pltpu.PrefetchScalarGridSpec(
            num_scalar_prefetch=1, grid=(rh // tr,), in_specs=specs,
            out_specs=pl.BlockSpec((None, tr, cols), lambda i, pos_ref: (pos_ref[0], i, 0))),
        out_shape=jax.ShapeDtypeStruct((2, rh, cols), F32), compiler_params=_params(),
    )(pos_arr, gath, gath, gath, gath)


def _small_allreduce(v, name, rider=None, after=()):
    n = v.shape[0]
    n_dev = 8

    def body(v_ref, o_ref, buf, send_sems, recv_sems):
        x, y, c = _mesh_pos()
        me = 4 * x + 2 * y + c
        buf[me] = v_ref[...]
        sends = []
        peers = []
        for r in range(1, n_dev):
            px = 1 - x if r & 4 else x
            py = 1 - y if r & 2 else y
            pc = 1 - c if r & 1 else c
            peers.append((px, py, pc))
            cp = pltpu.make_async_remote_copy(
                src_ref=v_ref, dst_ref=buf.at[me], send_sem=send_sems.at[r - 1],
                recv_sem=recv_sems.at[r - 1], device_id=(px, py, pc), device_id_type=MESH)
            cp.start()
            sends.append(cp)
        for r, (px, py, pc) in enumerate(peers):
            pltpu.make_async_remote_copy(
                src_ref=v_ref, dst_ref=buf.at[4 * px + 2 * py + pc], send_sem=send_sems.at[r],
                recv_sem=recv_sems.at[r], device_id=(px, py, pc), device_id_type=MESH).wait_recv()
        for cp in sends:
            cp.wait_send()
        acc = buf[0]
        for i in range(1, n_dev):
            acc = acc + buf[i]
        o_ref[...] = acc

    whole = pl.BlockSpec(v.shape, lambda i: (0, 0))
    return _pallas(
        body, name=name, grid=(1,), in_specs=[whole], out_specs=whole,
        out_shape=jax.ShapeDtypeStruct(v.shape, v.dtype), operands=[v],
        scratch_shapes=[pltpu.VMEM((n_dev, n, LANES), F32), pltpu.SemaphoreType.DMA((n_dev - 1,)),
                        pltpu.SemaphoreType.DMA((n_dev - 1,))],
        rider=rider, after=after)


def _adamw_math(w, g, m, v):
    m = ADAM_B1 * m + (1.0 - ADAM_B1) * g
    v = ADAM_B2 * v + (1.0 - ADAM_B2) * (g * g)
    m_hat = m / (1.0 - ADAM_B1 ** ADAM_STEP)
    v_hat = v / (1.0 - ADAM_B2 ** ADAM_STEP)
    delta = -ADAM_LR * (m_hat / (jnp.sqrt(v_hat) + ADAM_EPS) + ADAM_WD * w)
    return delta, m, v


def _adamw(w, g, m, v, name, after=()):
    r, c = w.shape
    tr = 128 if r % 128 == 0 else 64
    assert r % tr == 0

    def body(w_ref, g_ref, m_ref, v_ref, go_ref, d_ref, mo_ref, vo_ref):
        gv = g_ref[...]
        d, mn, vn = _adamw_math(w_ref[...], gv, m_ref[...], v_ref[...])
        go_ref[...] = gv
        d_ref[...] = d
        mo_ref[...] = mn
        vo_ref[...] = vn

    blk = pl.BlockSpec((tr, c), lambda i: (i, 0))
    return _pallas(body, name=name, grid=(r // tr,), in_specs=[blk] * 4, out_specs=[blk] * 4,
                   out_shape=[jax.ShapeDtypeStruct((r, c), F32)] * 4, operands=[w, g, m, v], after=after)


def _adamw_small(ws, gs, ms, vs, name):
    n = len(ws)

    def body(*refs):
        w_r, g_r, m_r, v_r = refs[:n], refs[n:2 * n], refs[2 * n:3 * n], refs[3 * n:4 * n]
        d_o, m_o, v_o = refs[4 * n:5 * n], refs[5 * n:6 * n], refs[6 * n:7 * n]
        for i in range(n):
            d, mn, vn = _adamw_math(w_r[i][...], g_r[i][...], m_r[i][...], v_r[i][...])
            d_o[i][...] = d
            m_o[i][...] = mn
            v_o[i][...] = vn

    specs = [pl.BlockSpec(w.shape, lambda i: (0, 0)) for w in ws]
    shapes = [jax.ShapeDtypeStruct(w.shape, F32) for w in ws]
    outs = pl.pallas_call(
        body, name=name, grid=(1,), in_specs=specs * 4, out_specs=specs * 3, out_shape=shapes * 3,
        compiler_params=_params(),
    )(*ws, *gs, *ms, *vs)
    return outs[:n], outs[n:2 * n], outs[2 * n:]


BIG = ("w_in", "w_o_attn", "w_pw_conv", "w_out", "w_ffn_in", "w_ffn_out")
ROW_SHARDED = ("w_out", "w_ffn_out")
SMALL = ("norm1_w", "b_gate", "q_norm_w", "k_norm_w", "conv_w", "conv_b", "conv_ln_w", "conv_ln_b", "norm2_w")
ORDER = ("norm1_w", "w_in", "b_gate", "q_norm_w", "k_norm_w", "w_o_attn", "conv_w", "conv_b", "conv_ln_w",
         "conv_ln_b", "w_pw_conv", "w_out", "norm2_w", "w_ffn_in", "w_ffn_out")
PACK_TILE = 8 * LANES


def _pack_small(parts):
    rows = []
    for p in parts:
        flat = p.reshape(-1)
        pad = (-flat.shape[0]) % PACK_TILE
        rows.append(jnp.pad(flat, (0, pad)).reshape(-1, LANES))
    return jnp.concatenate(rows, axis=0)


def _unpack_small(packed, shapes):
    out, row = [], 0
    for shp in shapes:
        size = int(np.prod(shp))
        nrow = -(-size // PACK_TILE) * (PACK_TILE // LANES)
        out.append(packed[row:row + nrow].reshape(-1)[:size].reshape(shp))
        row += nrow
    return out


def kernel(x, positions, norm1_w, w_in, b_gate, q_norm_w, k_norm_w, w_o_attn, conv_w, conv_b, conv_ln_w, conv_ln_b, w_pw_conv, w_out, norm2_w, w_ffn_in, w_ffn_out, loss_target, m_norm1_w, m_w_in, m_b_gate, m_q_norm_w, m_k_norm_w, m_w_o_attn, m_conv_w, m_conv_b, m_conv_ln_w, m_conv_ln_b, m_w_pw_conv, m_w_out, m_norm2_w, m_w_ffn_in, m_w_ffn_out, v_norm1_w, v_w_in, v_b_gate, v_q_norm_w, v_k_norm_w, v_w_o_attn, v_conv_w, v_conv_b, v_conv_ln_w, v_conv_ln_b, v_w_pw_conv, v_w_out, v_norm2_w, v_w_ffn_in, v_w_ffn_out):
    w = dict(norm1_w=norm1_w, w_in=w_in, b_gate=b_gate, q_norm_w=q_norm_w, k_norm_w=k_norm_w, w_o_attn=w_o_attn,
             conv_w=conv_w, conv_b=conv_b, conv_ln_w=conv_ln_w, conv_ln_b=conv_ln_b, w_pw_conv=w_pw_conv,
             w_out=w_out, norm2_w=norm2_w, w_ffn_in=w_ffn_in, w_ffn_out=w_ffn_out)
    m = dict(norm1_w=m_norm1_w, w_in=m_w_in, b_gate=m_b_gate, q_norm_w=m_q_norm_w, k_norm_w=m_k_norm_w,
             w_o_attn=m_w_o_attn, conv_w=m_conv_w, conv_b=m_conv_b, conv_ln_w=m_conv_ln_w,
             conv_ln_b=m_conv_ln_b, w_pw_conv=m_w_pw_conv, w_out=m_w_out, norm2_w=m_norm2_w,
             w_ffn_in=m_w_ffn_in, w_ffn_out=m_w_ffn_out)
    v = dict(norm1_w=v_norm1_w, w_in=v_w_in, b_gate=v_b_gate, q_norm_w=v_q_norm_w, k_norm_w=v_k_norm_w,
             w_o_attn=v_w_o_attn, conv_w=v_conv_w, conv_b=v_conv_b, conv_ln_w=v_conv_ln_w,
             conv_ln_b=v_conv_ln_b, w_pw_conv=v_w_pw_conv, w_out=v_w_out, norm2_w=v_norm2_w,
             w_ffn_in=v_w_ffn_in, w_ffn_out=v_w_ffn_out)
    cx, cy, cc = _mesh_pos()
    chip = 2 * cx + cy

    chip_arr = chip.reshape(1).astype(jnp.int32)
    pos_arr = jnp.stack([cc, chip]).astype(jnp.int32)
    bufs = {}
    for n in BIG:
        buf = _cast_into_slot(w[n][0], chip_arr, BF16, f"cast_{n}")
        bufs[n] = buf.reshape(N_CHIPS, 2, buf.shape[1] // 2, buf.shape[2])
    small_bufs = [_cast_into_slot(w[n][0], chip_arr, F32, f"slot_{n}") for n in ("conv_w", "b_gate")]
    w_in_buf, conv_w_buf, b_gate_buf = _comm_call(_gather_both_legs_rider([bufs["w_in"]], small_bufs),
                                                  "allgather_w_in")
    wts = dict(w_in=w_in_buf.reshape(N_CHIPS, -1, w_in_buf.shape[3]),
               conv_w=conv_w_buf.transpose(1, 0, 2).reshape(CONV_WIDTH, -1),
               b_gate=b_gate_buf.transpose(1, 0, 2).reshape(2, 1, -1),
               norm1_w=norm1_w, q_norm_w=q_norm_w, k_norm_w=k_norm_w, conv_b=conv_b, conv_ln_w=conv_ln_w,
               conv_ln_b=conv_ln_b, norm2_w=norm2_w)

    loss, grad_x, g, reduced, w_in_in_flight = _forward_backward(
        x[0], positions.reshape(-1, 1), loss_target[0], wts, [bufs[n] for n in LATE_GATHER], pos_arr)
    grads = {n: b.reshape(-1, b.shape[2]) for n, b in reduced.items()}

    w_in_in_flight, started = w_in_in_flight
    delta, new_m, new_v = {}, {}, {}
    for n in EARLY_REDUCE:
        grads[n], delta[n], new_m[n], new_v[n] = _adamw(w[n][0], grads[n], m[n][0], v[n][0], f"adamw_{n}",
                                                        after=[started])
    small_parts = [loss] + [g[n] for n in SMALL]
    small_shapes = [p.shape for p in small_parts]
    summed = _small_allreduce(_pack_small(small_parts), "small_allreduce",
                              after=[delta[n] for n in EARLY_REDUCE])
    reduced = _unpack_small(summed, small_shapes)
    loss_total = reduced[0].reshape(())
    for n, r in zip(SMALL, reduced[1:]):
        grads[n] = r
    ch_shard = conv_w.shape[2]
    grads["conv_w"] = lax.dynamic_slice_in_dim(grads["conv_w"], chip * ch_shard, ch_shard, axis=1)
    d_shard = b_gate.shape[2]
    grads["b_gate"] = lax.dynamic_slice_in_dim(grads["b_gate"], chip * d_shard, d_shard, axis=1)

    (by_chip_w_in,), _ = _split_wait(w_in_in_flight, after=[delta[n] for n in EARLY_REDUCE] + [summed],
                                     name="grads_w_in_exchange_wait")
    half_w_in = _sum_chips(by_chip_w_in, pos_arr, "grads_chip_sum_w_in")
    (shard_w_in,) = _comm_call(_pair_gather_rider([half_w_in]), "grads_pair_gather_w_in")
    grads["w_in"], delta["w_in"], new_m["w_in"], new_v["w_in"] = _adamw(
        w["w_in"][0], shard_w_in.reshape(-1, shard_w_in.shape[2]), m["w_in"][0], v["w_in"][0], "adamw_w_in")
    flat2 = lambda a: a.reshape(-1, a.shape[-1])
    d_s, m_s, v_s = _adamw_small([flat2(w[n]) for n in SMALL], [flat2(grads[n]) for n in SMALL],
                                 [flat2(m[n]) for n in SMALL], [flat2(v[n]) for n in SMALL], "adamw_small")
    for i, n in enumerate(SMALL):
        delta[n], new_m[n], new_v[n] = d_s[i], m_s[i], v_s[i]

    shaped = lambda d, n: d[n].reshape(w[n].shape)
    return (loss_total, grad_x[None], *[shaped(grads, n) for n in ORDER], *[shaped(delta, n) for n in ORDER],
            *[shaped(new_m, n) for n in ORDER], *[shaped(new_v, n) for n in ORDER])
```

```python
import functools

import numpy as np
import jax
import jax.numpy as jnp
from jax import lax
from jax.experimental import pallas as pl
from jax.experimental.pallas import tpu as pltpu

F32 = jnp.float32
BF16 = jnp.bfloat16
MESH = pl.DeviceIdType.MESH
ANY = pl.BlockSpec(memory_space=pl.ANY)

HEAD_DIM = 64
N_SLOT_HEADS = 8
DILATIONS = (1, 4, 16)
HALF_SPAN = 64
ROPE_THETA = 500000.0
ROT_DIM = 16
CONV_WIDTH = 31
EPS = 1e-6
NEG_INF = -1e30
ADAM_LR, ADAM_B1, ADAM_B2, ADAM_EPS, ADAM_WD, ADAM_STEP = 0.001, 0.9, 0.999, 1e-08, 0.01, 10

LANES = 128
QBLK = 128
KWIN = QBLK + 2 * HALF_SPAN
VMEM_LIMIT = 48 * 1024 * 1024
N_CHIPS = 4


def _params(**kw):
    return pltpu.CompilerParams(vmem_limit_bytes=VMEM_LIMIT, **kw)


class _Rider:
    def __init__(self, operands, out_shapes, aliases, scratch, start, wait):
        self.operands, self.out_shapes, self.aliases = list(operands), list(out_shapes), dict(aliases)
        self.scratch, self.start, self.wait = list(scratch), start, wait


def _pallas(body, *, name, grid, in_specs, out_specs, out_shape, operands, scratch_shapes=(), aliases=None,
            rider=None, after=()):
    single = not isinstance(out_specs, (list, tuple))
    out_specs_l = [out_specs] if single else list(out_specs)
    out_shape_l = [out_shape] if single else list(out_shape)
    aliases = dict(aliases or {})

    def call(fn, all_in_specs, all_out_specs, all_out_shape, all_scratch, all_aliases, all_operands):
        return pl.pallas_call(
            fn, name=name, grid=grid, in_specs=all_in_specs, out_specs=all_out_specs, out_shape=all_out_shape,
            scratch_shapes=all_scratch, input_output_aliases=all_aliases, compiler_params=_params(),
        )(*all_operands)

    if rider is None:
        n_main = len(in_specs)

        def ordered(*refs):
            body(*refs[:n_main], *refs[n_main + len(after):])

        res = call(ordered if after else body, list(in_specs) + [ANY] * len(after), out_specs_l, out_shape_l,
                   list(scratch_shapes), aliases, list(operands) + list(after))
        return res[0] if single else res
    assert not after
    n_in, n_rin = len(in_specs), len(rider.operands)
    n_out, n_rout = len(out_specs_l), len(rider.out_shapes)
    n_sc = len(scratch_shapes)

    def wrapped(*refs):
        main_in, r_in = refs[:n_in], refs[n_in:n_in + n_rin]
        o0 = n_in + n_rin
        main_out, r_out = refs[o0:o0 + n_out], refs[o0 + n_out:o0 + n_out + n_rout]
        s0 = o0 + n_out + n_rout
        main_sc, r_sc = refs[s0:s0 + n_sc], refs[s0 + n_sc:]
        ids = [pl.program_id(d) for d in range(len(grid))]
        first = functools.reduce(jnp.logical_and, [i == 0 for i in ids])
        last = functools.reduce(jnp.logical_and, [i == n - 1 for i, n in zip(ids, grid)])

        @pl.when(first)
        def _():
            rider.start(r_in, r_out, r_sc)

        body(*main_in, *main_out, *main_sc)

        @pl.when(last)
        def _():
            rider.wait(r_in, r_out, r_sc)

    for src, dst in rider.aliases.items():
        aliases[n_in + src] = n_out + dst
    res = call(wrapped, list(in_specs) + [ANY] * n_rin, out_specs_l + [ANY] * n_rout,
               out_shape_l + rider.out_shapes, list(scratch_shapes) + rider.scratch, aliases,
               list(operands) + rider.operands)
    main = res[:n_out]
    return (main[0] if single else main), res[n_out:]


def _matmul(a, b, *, mode, tm, tn, tk, out_dtype, name, b_blocked=False,
            out_blocked=None, cols_outer=False, rider=None, after=()):
    a_shape = a.shape
    if mode == "nn":
        m_dim, k_dim = a_shape
        n_dim = b.shape[0] * b.shape[2] if b_blocked else b.shape[1]
        rows, cols, red = m_dim, n_dim, k_dim
    elif mode == "nt":
        m_dim, n_dim = a_shape
        k_dim = b.shape[1] if b_blocked else b.shape[0]
        rows, cols, red = m_dim, k_dim, n_dim
    else:
        m_dim, k_dim = a_shape
        n_dim = b.shape[1]
        rows, cols, red = k_dim, n_dim, m_dim
    assert rows % tm == 0 and cols % tn == 0 and red % tk == 0, (name, rows, cols, red)
    ni, nj, nk = rows // tm, cols // tn, red // tk

    if mode == "nn":
        a_spec = pl.BlockSpec((tm, tk), lambda i, j, k: (i, k))
        if b_blocked:
            per = b.shape[2] // tn
            b_spec = pl.BlockSpec((None, tk, tn), lambda i, j, k: (j // per, k, j % per))
        else:
            b_spec = pl.BlockSpec((tk, tn), lambda i, j, k: (k, j))
        dims = (((1,), (0,)), ((), ()))
    elif mode == "nt":
        a_spec = pl.BlockSpec((tm, tk), lambda i, j, k: (i, k))
        if b_blocked:
            per = b.shape[2] // tk
            b_spec = pl.BlockSpec((None, tn, tk), lambda i, j, k: (k // per, j, k % per))
        else:
            b_spec = pl.BlockSpec((tn, tk), lambda i, j, k: (j, k))
        dims = (((1,), (1,)), ((), ()))
    else:
        a_spec = pl.BlockSpec((tk, tm), lambda i, j, k: (k, i))
        b_spec = pl.BlockSpec((tk, tn), lambda i, j, k: (k, j))
        dims = (((0,), (0,)), ((), ()))

    if out_blocked:
        per_o = (cols // out_blocked) // tn
        out_spec = pl.BlockSpec((None, tm, tn), lambda i, j, k: (j // per_o, i, j % per_o))
        out_shape = jax.ShapeDtypeStruct((out_blocked, rows, cols // out_blocked), out_dtype)
    else:
        out_spec = pl.BlockSpec((tm, tn), lambda i, j, k: (i, j))
        out_shape = jax.ShapeDtypeStruct((rows, cols), out_dtype)

    def body(a_ref, b_ref, o_ref, *acc):
        prod = lax.dot_general(a_ref[...], b_ref[...], dims, preferred_element_type=F32)
        if nk == 1:
            o_ref[...] = prod.astype(out_dtype)
        else:
            acc_ref, = acc
            k = pl.program_id(2)

            @pl.when(k == 0)
            def _():
                acc_ref[...] = prod

            @pl.when(k > 0)
            def _():
                acc_ref[...] += prod

            @pl.when(k == nk - 1)
            def _():
                o_ref[...] = acc_ref[...].astype(out_dtype)

    scratch = [pltpu.VMEM((tm, tn), F32)] if nk > 1 else []
    grid = (ni, nj, nk)
    if cols_outer:
        swap = lambda spec: pl.BlockSpec(spec.block_shape, lambda j, i, k, f=spec.index_map: f(i, j, k))
        a_spec, b_spec, out_spec, grid = swap(a_spec), swap(b_spec), swap(out_spec), (nj, ni, nk)
    return _pallas(body, name=name, grid=grid, in_specs=[a_spec, b_spec], out_specs=out_spec,
                   out_shape=out_shape, operands=[a, b], scratch_shapes=scratch, rider=rider, after=after)


def _rmsnorm_fwd(x, w, name):
    s, d = x.shape
    tm = 256

    def body(x_ref, w_ref, o_ref):
        xv = x_ref[...]
        rstd = lax.rsqrt(jnp.mean(xv * xv, axis=-1, keepdims=True) + EPS)
        o_ref[...] = (xv * rstd * w_ref[...]).astype(BF16)

    return pl.pallas_call(
        body, name=name, grid=(s // tm,),
        in_specs=[pl.BlockSpec((tm, d), lambda i: (i, 0)), pl.BlockSpec((1, d), lambda i: (0, 0))],
        out_specs=pl.BlockSpec((tm, d), lambda i: (i, 0)),
        out_shape=jax.ShapeDtypeStruct((s, d), BF16), compiler_params=_params(),
    )(x, w)


def _rmsnorm_bwd(dh, x, w, dres, name, rider=None):
    s, d = x.shape
    tm = 256

    def body(dh_ref, x_ref, w_ref, dres_ref, dx_ref, dxb_ref, dw_ref):
        xv = x_ref[...]
        rstd = lax.rsqrt(jnp.mean(xv * xv, axis=-1, keepdims=True) + EPS)
        xhat = xv * rstd
        dhv = dh_ref[...]
        g = dhv * w_ref[...]
        dx = rstd * (g - xhat * jnp.mean(g * xhat, axis=-1, keepdims=True)) + dres_ref[...]
        dx_ref[...] = dx
        dxb_ref[...] = dx.astype(BF16)
        part = jnp.sum(dhv * xhat, axis=0, keepdims=True)

        @pl.when(pl.program_id(0) == 0)
        def _():
            dw_ref[...] = part

        @pl.when(pl.program_id(0) > 0)
        def _():
            dw_ref[...] += part

    row = pl.BlockSpec((tm, d), lambda i: (i, 0))
    vec = pl.BlockSpec((1, d), lambda i: (0, 0))
    return _pallas(
        body, name=name, grid=(s // tm,), in_specs=[row, row, vec, row], out_specs=[row, row, vec],
        out_shape=[jax.ShapeDtypeStruct((s, d), F32), jax.ShapeDtypeStruct((s, d), BF16),
                   jax.ShapeDtypeStruct((1, d), F32)],
        operands=[dh, x, w, dres], rider=rider)


def _rope_consts():
    lane = np.arange(LANES)
    in_head = lane % HEAD_DIM
    inv_freq = ROPE_THETA ** (-jnp.arange(0, ROT_DIM, 2, dtype=F32) / ROT_DIM)
    invf = jnp.where(jnp.asarray(in_head < ROT_DIM), jnp.tile(inv_freq, LANES // (ROT_DIM // 2)), 0.0)
    m_a = np.where(in_head < ROT_DIM // 2, -1.0, 0.0).astype(np.float32)
    m_b = np.where((in_head >= ROT_DIM // 2) & (in_head < ROT_DIM), 1.0, 0.0).astype(np.float32)
    block_diag = (lane[:, None] // HEAD_DIM == lane[None, :] // HEAD_DIM).astype(np.float32)
    return (invf.reshape(1, LANES).astype(F32), jnp.asarray(m_a).reshape(1, LANES),
            jnp.asarray(m_b).reshape(1, LANES), jnp.asarray(block_diag, dtype=BF16))


def _head_sums(v, bd):
    hi = v.astype(BF16)
    lo = (v - hi.astype(F32)).astype(BF16)
    return jnp.dot(hi, bd, preferred_element_type=F32) + jnp.dot(lo, bd, preferred_element_type=F32)


def _qk_fwd(proj, pos_col, qw2, kw2, consts, name, rider=None):
    s = proj.shape[0]
    width = 3 * N_SLOT_HEADS * HEAD_DIM
    tm = 128
    invf, m_a, m_b, bd = consts
    scale = HEAD_DIM ** -0.5

    def body(q_ref, k_ref, pos_ref, qw_ref, kw_ref, invf_ref, ma_ref, mb_ref, bd_ref, qo_ref, ko_ref):
        ang = pos_ref[...].astype(F32) * invf_ref[...]
        cos = jnp.cos(ang)
        sin = jnp.sin(ang)
        s_a = sin * ma_ref[...]
        s_b = sin * mb_ref[...]
        bdv = bd_ref[...]
        for src, w_ref, dst, sc in ((q_ref, qw_ref, qo_ref, scale), (k_ref, kw_ref, ko_ref, 1.0)):
            for cb in range(width // LANES):
                cols = slice(cb * LANES, (cb + 1) * LANES)
                t = src[:, cols]
                rstd = lax.rsqrt(_head_sums(t * t, bdv) * (1.0 / HEAD_DIM) + EPS)
                y = t * rstd * w_ref[...]
                r = y * cos + pltpu.roll(y, LANES - 8, axis=1) * s_a + pltpu.roll(y, 8, axis=1) * s_b
                dst[:, cols] = r * sc if sc != 1.0 else r

    vec = pl.BlockSpec((1, LANES), lambda i: (0, 0))
    return _pallas(
        body, name=name, grid=(s // tm,),
        in_specs=[pl.BlockSpec((tm, width), lambda i: (i, 0)), pl.BlockSpec((tm, width), lambda i: (i, 1)),
                  pl.BlockSpec((tm, 1), lambda i: (i, 0)), vec, vec, vec, vec, vec,
                  pl.BlockSpec((LANES, LANES), lambda i: (0, 0))],
        out_specs=[pl.BlockSpec((tm, width), lambda i: (i, 0))] * 2,
        out_shape=[jax.ShapeDtypeStruct((s, width), F32)] * 2,
        operands=[proj, proj, pos_col, qw2, kw2, invf, m_a, m_b, bd], rider=rider)


def _qk_bwd(dqn, dkn, dv, da, db, dgl, proj, pos_col, qw2, kw2, consts, name, rider=None):
    s = proj.shape[0]
    width = 3 * N_SLOT_HEADS * HEAD_DIM
    ch = da.shape[1]
    gate_w = dgl.shape[1]
    out_w = 3 * width + 2 * ch + gate_w
    assert out_w == proj.shape[1]
    tm = 128
    invf, m_a, m_b, bd = consts
    scale = HEAD_DIM ** -0.5

    def body(dq_ref, dk_ref, dv_ref, da_ref, db_ref, dgl_ref, q_ref, k_ref, pos_ref, qw_ref, kw_ref,
             invf_ref, ma_ref, mb_ref, bd_ref, out_ref, dqw_ref, dkw_ref):
        ang = pos_ref[...].astype(F32) * invf_ref[...]
        cos = jnp.cos(ang)
        sin = jnp.sin(ang)
        s_a = sin * ma_ref[...]
        s_b = sin * mb_ref[...]
        bdv = bd_ref[...]
        first = pl.program_id(0) == 0
        for src, dsrc, w_ref, col0, dw_ref, sc in ((q_ref, dq_ref, qw_ref, 0, dqw_ref, scale),
                                                   (k_ref, dk_ref, kw_ref, width, dkw_ref, 1.0)):
            dw_acc = jnp.zeros((1, LANES), F32)
            for cb in range(width // LANES):
                cols = slice(cb * LANES, (cb + 1) * LANES)
                t = src[:, cols]
                dr = dsrc[:, cols]
                if sc != 1.0:
                    dr = dr * sc
                dy = dr * cos + pltpu.roll(dr * s_a, 8, axis=1) + pltpu.roll(dr * s_b, LANES - 8, axis=1)
                rstd = lax.rsqrt(_head_sums(t * t, bdv) * (1.0 / HEAD_DIM) + EPS)
                xhat = t * rstd
                g = dy * w_ref[...]
                dt = rstd * (g - xhat * (_head_sums(g * xhat, bdv) * (1.0 / HEAD_DIM)))
                out_ref[:, col0 + cb * LANES: col0 + (cb + 1) * LANES] = dt.astype(BF16)
                dw_acc = dw_acc + jnp.sum(dy * xhat, axis=0, keepdims=True)
            dw_acc = dw_acc + pltpu.roll(dw_acc, HEAD_DIM, axis=1)

            @pl.when(first)
            def _(dw_ref=dw_ref, dw_acc=dw_acc):
                dw_ref[...] = dw_acc

            @pl.when(jnp.logical_not(first))
            def _(dw_ref=dw_ref, dw_acc=dw_acc):
                dw_ref[...] += dw_acc
        out_ref[:, 2 * width: 3 * width] = dv_ref[...].astype(BF16)
        out_ref[:, 3 * width: 3 * width + ch] = da_ref[...]
        out_ref[:, 3 * width + ch: 3 * width + 2 * ch] = db_ref[...]
        out_ref[:, 3 * width + 2 * ch: out_w] = dgl_ref[...]

    vec = pl.BlockSpec((1, LANES), lambda i: (0, 0))
    blk = lambda c: pl.BlockSpec((tm, width), lambda i: (i, c))
    cblk = pl.BlockSpec((tm, ch), lambda i: (i, 0))
    return _pallas(
        body, name=name, grid=(s // tm,),
        in_specs=[blk(0), blk(0), blk(0), cblk, cblk, pl.BlockSpec((tm, gate_w), lambda i: (i, 0)),
                  blk(0), blk(1), pl.BlockSpec((tm, 1), lambda i: (i, 0)), vec, vec, vec, vec, vec,
                  pl.BlockSpec((LANES, LANES), lambda i: (0, 0))],
        out_specs=[pl.BlockSpec((tm, out_w), lambda i: (i, 0)), vec, vec],
        out_shape=[jax.ShapeDtypeStruct((s, out_w), BF16)] + [jax.ShapeDtypeStruct((1, LANES), F32)] * 2,
        operands=[dqn, dkn, dv, da, db, dgl, proj, proj, pos_col, qw2, kw2, invf, m_a, m_b, bd],
        rider=rider)


def _row_chunks(n_rows, fn, chunk=256):
    def step(i, c):
        fn(pl.ds(pl.multiple_of(i * chunk, chunk), chunk))
        return c
    lax.fori_loop(0, n_rows // chunk, step, 0)


def _to_residue_major(dst, src, s, d, dst_off=0, cast=None):
    seq = s // d
    for r in range(d):
        v = src[...] if d == 1 else src[pl.ds(r, seq, stride=d), :]
        dst[dst_off + r * seq: dst_off + (r + 1) * seq, :] = v if cast is None else v.astype(cast)


def _from_residue_major(dst, src, s, d, src_off=0):
    seq = s // d
    for r in range(d):
        v = src[src_off + r * seq: src_off + (r + 1) * seq, :]
        if d == 1:
            dst[...] = v
        else:
            dst[pl.ds(r, seq, stride=d), :] = v


def _band_bias():
    qi = lax.broadcasted_iota(jnp.int32, (QBLK, KWIN), 0)
    kj = lax.broadcasted_iota(jnp.int32, (QBLK, KWIN), 1)
    return jnp.where(jnp.abs(kj - HALF_SPAN - qi) <= HALF_SPAN, 0.0, NEG_INF).astype(F32)


def _range_bias(base, seq):
    kj = lax.broadcasted_iota(jnp.int32, (1, KWIN), 1)
    lo = (base & -seq) - base + HALF_SPAN
    return jnp.where((kj >= lo) & (kj < lo + seq), 0.0, NEG_INF).astype(F32)


def _skewed_blocks(n_blk, produce, consume):
    produce(0, 0)
    for b in range(n_blk):
        consume(b, b % 2)
        if b + 1 < n_blk:
            produce(b + 1, (b + 1) % 2)


def _block_base(b):
    return b * QBLK if isinstance(b, int) else pl.multiple_of(b * QBLK, QBLK)


def _attn_fwd(qn, kn, proj, name, rider=None):
    s = qn.shape[0]
    n_pairs = N_SLOT_HEADS * HEAD_DIM // LANES
    v_col0 = 2 * qn.shape[1] // LANES
    nt_dims = (((1,), (1,)), ((), ()))

    def body(q_ref, k_ref, v_ref, attn_ref, lse_ref, attn_b_ref, q_rm, k_rm, v_rm, acc_rm, m_rm, l_rm,
             acc_p, m_p, l_p, m_run, l_run, acc_run, band, s_buf, m_buf):
        g = pl.program_id(1)
        zpad = jnp.zeros((HALF_SPAN, LANES), BF16)
        k_rm[0:HALF_SPAN, :] = zpad
        k_rm[s + HALF_SPAN: s + 2 * HALF_SPAN, :] = zpad
        v_rm[0:HALF_SPAN, 0:LANES] = zpad
        v_rm[s + HALF_SPAN: s + 2 * HALF_SPAN, 0:LANES] = zpad

        def ones_rows(rows):
            v_rm[pl.ds(rows.start, rows.size), LANES:2 * LANES] = jnp.ones((rows.size, LANES), BF16)

        _row_chunks(s + 2 * HALF_SPAN, ones_rows, chunk=2 * HALF_SPAN)
        band[...] = _band_bias()
        lane = lax.broadcasted_iota(jnp.int32, (QBLK, LANES), 1)
        low = lane < HEAD_DIM
        n_blk = s // QBLK

        for gi, d in enumerate(DILATIONS):
            @pl.when(g == gi)
            def _(gi=gi, d=d):
                seq = s // d
                _to_residue_major(q_rm, q_ref, s, d, cast=BF16)
                _to_residue_major(k_rm, k_ref, s, d, dst_off=HALF_SPAN, cast=BF16)
                _to_residue_major(v_rm.at[:, 0:LANES], v_ref, s, d, dst_off=HALF_SPAN, cast=BF16)

                def scores(b, slot):
                    base = _block_base(b)
                    q = q_rm[pl.ds(base, QBLK), :]
                    zero = jnp.zeros_like(q)
                    q2 = jnp.concatenate([jnp.where(low, q, zero), jnp.where(low, zero, q)], axis=0)
                    sc = lax.dot_general(q2, k_rm[pl.ds(base, KWIN), :], nt_dims, preferred_element_type=F32)
                    bias = band[...] + _range_bias(base, seq)
                    for hh in range(2):
                        rows = slice(hh * QBLK, (hh + 1) * QBLK)
                        sh = sc[rows, :] + bias
                        s_buf[slot, rows, :] = sh
                        m_buf[slot, rows, :] = jnp.broadcast_to(jnp.max(sh, axis=-1, keepdims=True), (QBLK, LANES))

                def outputs(b, slot):
                    base = _block_base(b)
                    sv = s_buf[slot]
                    mb = m_buf[slot]
                    p = jnp.exp(jnp.concatenate([sv[:, 0:LANES] - mb, sv[:, LANES:2 * LANES] - mb], axis=1))
                    pv = jnp.dot(p.astype(BF16), v_rm[pl.ds(base, KWIN), :], preferred_element_type=F32)
                    rows = pl.ds(base, QBLK)
                    acc_rm[rows, :] = jnp.where(low, pv[0:QBLK, 0:LANES], pv[QBLK:2 * QBLK, 0:LANES])
                    l_rm[rows, :] = jnp.where(low, pv[0:QBLK, LANES:2 * LANES], pv[QBLK:2 * QBLK, LANES:2 * LANES])
                    m_rm[rows, :] = jnp.where(low, mb[0:QBLK, :], mb[QBLK:2 * QBLK, :])

                _skewed_blocks(n_blk, scores, outputs)
                if d == 1:
                    src = (acc_rm, m_rm, l_rm)
                else:
                    for dst_, src_ in ((acc_p, acc_rm), (m_p, m_rm), (l_p, l_rm)):
                        _from_residue_major(dst_, src_, s, d)
                    src = (acc_p, m_p, l_p)

                def combine(rows):
                    a_g, m_g, l_g = src[0][rows, :], src[1][rows, :], src[2][rows, :]
                    if gi == 0:
                        m_new, l_new, a_new = m_g, l_g, a_g
                    else:
                        m_old = m_run[rows, :]
                        m_new = jnp.maximum(m_old, m_g)
                        w_old = jnp.exp(m_old - m_new)
                        w_g = jnp.exp(m_g - m_new)
                        l_new = l_run[rows, :] * w_old + l_g * w_g
                        a_new = acc_run[rows, :] * w_old + a_g * w_g
                    if gi == len(DILATIONS) - 1:
                        out = a_new / l_new
                        attn_ref[rows, :] = out
                        attn_b_ref[rows, :] = out.astype(BF16)
                        lse_ref[rows, :] = m_new + jnp.log(l_new)
                    else:
                        m_run[rows, :] = m_new
                        l_run[rows, :] = l_new
                        acc_run[rows, :] = a_new

                _row_chunks(s, combine)

    qk_spec = pl.BlockSpec((s, LANES), lambda hp, g: (0, g * n_pairs + hp))
    v_spec = pl.BlockSpec((s, LANES), lambda hp, g: (0, v_col0 + g * n_pairs + hp))
    o_spec = pl.BlockSpec((s, LANES), lambda hp, g: (0, hp))
    f32buf = pltpu.VMEM((s, LANES), F32)
    return _pallas(
        body, name=name, grid=(n_pairs, len(DILATIONS)), in_specs=[qk_spec, qk_spec, v_spec],
        out_specs=[o_spec, o_spec, o_spec],
        out_shape=[jax.ShapeDtypeStruct((s, n_pairs * LANES), F32)] * 2
        + [jax.ShapeDtypeStruct((s, n_pairs * LANES), BF16)],
        operands=[qn, kn, proj],
        scratch_shapes=[pltpu.VMEM((s, LANES), BF16), pltpu.VMEM((s + 2 * HALF_SPAN, LANES), BF16),
                        pltpu.VMEM((s + 2 * HALF_SPAN, 2 * LANES), BF16)] + [f32buf] * 9
        + [pltpu.VMEM((QBLK, KWIN), F32), pltpu.VMEM((2, 2 * QBLK, KWIN), F32),
           pltpu.VMEM((2, 2 * QBLK, LANES), F32)],
        rider=rider)


def _attn_bwd(qn, kn, proj, dattn, attn, lse, bd, name, rider=None):
    s = qn.shape[0]
    n_pairs = N_SLOT_HEADS * HEAD_DIM // LANES
    v_col0 = 2 * qn.shape[1] // LANES
    nt_dims = (((1,), (1,)), ((), ()))
    tn_dims = (((0,), (0,)), ((), ()))
    spad = s + 2 * HALF_SPAN

    def body(q_ref, k_ref, v_ref, do_ref, o_ref, lse_ref, bd_ref, dq_ref, dk_ref, dv_ref,
             q_rm, k_rm, v_rm, do_rm, lse0_rm, lse1_rm, dd0_rm, dd1_rm, dq_rm, dk_rm, dv_rm,
             lse0_p, lse1_p, dd0_p, dd1_p, band, p_buf, ds_buf):
        g = pl.program_id(1)
        zpad = jnp.zeros((HALF_SPAN, LANES), BF16)
        for buf in (k_rm, v_rm):
            buf[0:HALF_SPAN, :] = zpad
            buf[s + HALF_SPAN: spad, :] = zpad
        zf = jnp.zeros((HALF_SPAN, LANES), F32)
        for buf in (dk_rm, dv_rm):
            buf[0:HALF_SPAN, :] = zf
            buf[s + HALF_SPAN: spad, :] = zf
        band[...] = _band_bias()

        def clear(rows):
            z = jnp.zeros((rows.size, LANES), F32)
            dk_rm[pl.ds(rows.start + HALF_SPAN, rows.size), :] = z
            dv_rm[pl.ds(rows.start + HALF_SPAN, rows.size), :] = z

        _row_chunks(s, clear)

        def prepare(rows):
            lo = lax.broadcasted_iota(jnp.int32, (rows.size, LANES), 1) < HEAD_DIM
            dsum = _head_sums(do_ref[rows, :] * o_ref[rows, :], bd_ref[...])
            dswap = pltpu.roll(dsum, HEAD_DIM, axis=1)
            dd0_p[rows, :] = jnp.where(lo, dsum, dswap)
            dd1_p[rows, :] = jnp.where(lo, dswap, dsum)
            lv = lse_ref[rows, :]
            lswap = pltpu.roll(lv, HEAD_DIM, axis=1)
            lse0_p[rows, :] = jnp.where(lo, lv, lswap)
            lse1_p[rows, :] = jnp.where(lo, lswap, lv)

        @pl.when(g == 0)
        def _():
            _row_chunks(s, prepare)
        lane = lax.broadcasted_iota(jnp.int32, (QBLK, LANES), 1)
        low = lane < HEAD_DIM
        n_blk = s // QBLK

        def stacked(ref, rows):
            val = ref[rows, :]
            zero = jnp.zeros_like(val)
            return jnp.concatenate([jnp.where(low, val, zero), jnp.where(low, zero, val)], axis=0)

        for gi, d in enumerate(DILATIONS):
            @pl.when(g == gi)
            def _(d=d):
                seq = s // d
                _to_residue_major(q_rm, q_ref, s, d, cast=BF16)
                _to_residue_major(k_rm, k_ref, s, d, dst_off=HALF_SPAN, cast=BF16)
                _to_residue_major(v_rm, v_ref, s, d, dst_off=HALF_SPAN, cast=BF16)
                _to_residue_major(do_rm, do_ref, s, d, cast=BF16)
                for dst_, src_ in ((lse0_rm, lse0_p), (lse1_rm, lse1_p), (dd0_rm, dd0_p), (dd1_rm, dd1_p)):
                    _to_residue_major(dst_, src_, s, d)

                def scores(b, slot):
                    base = _block_base(b)
                    rows = pl.ds(base, QBLK)
                    win = pl.ds(base, KWIN)
                    sc = lax.dot_general(stacked(q_rm, rows), k_rm[win, :], nt_dims, preferred_element_type=F32)
                    dp = lax.dot_general(stacked(do_rm, rows), v_rm[win, :], nt_dims, preferred_element_type=F32)
                    bias = band[...] + _range_bias(base, seq)
                    for hh, (lse_r, dd_r) in enumerate(((lse0_rm, dd0_rm), (lse1_rm, dd1_rm))):
                        r = slice(hh * QBLK, (hh + 1) * QBLK)
                        lse_h = lse_r[rows, :]
                        dd_h = dd_r[rows, :]
                        sh = sc[r, :] + bias
                        p = jnp.exp(jnp.concatenate([sh[:, 0:LANES] - lse_h, sh[:, LANES:KWIN] - lse_h], axis=1))
                        dph = dp[r, :]
                        ds = p * jnp.concatenate([dph[:, 0:LANES] - dd_h, dph[:, LANES:KWIN] - dd_h], axis=1)
                        p_buf[slot, r, :] = p.astype(BF16)
                        ds_buf[slot, r, :] = ds.astype(BF16)

                def grads(b, slot):
                    base = _block_base(b)
                    rows = pl.ds(base, QBLK)
                    win = pl.ds(base, KWIN)
                    p = p_buf[slot]
                    ds = ds_buf[slot]
                    dq2 = jnp.dot(ds, k_rm[win, :], preferred_element_type=F32)
                    dq_rm[rows, :] = jnp.where(low, dq2[0:QBLK, :], dq2[QBLK:2 * QBLK, :])
                    dk_rm[win, :] += lax.dot_general(ds, stacked(q_rm, rows), tn_dims, preferred_element_type=F32)
                    dv_rm[win, :] += lax.dot_general(p, stacked(do_rm, rows), tn_dims, preferred_element_type=F32)

                _skewed_blocks(n_blk, scores, grads)
                _from_residue_major(dq_ref, dq_rm, s, d)
                _from_residue_major(dk_ref, dk_rm, s, d, src_off=HALF_SPAN)
                _from_residue_major(dv_ref, dv_rm, s, d, src_off=HALF_SPAN)

    qk_spec = pl.BlockSpec((s, LANES), lambda hp, g: (0, g * n_pairs + hp))
    v_spec = pl.BlockSpec((s, LANES), lambda hp, g: (0, v_col0 + g * n_pairs + hp))
    o_spec = pl.BlockSpec((s, LANES), lambda hp, g: (0, hp))
    width = qn.shape[1]
    f32buf = pltpu.VMEM((s, LANES), F32)
    f32pad = pltpu.VMEM((spad, LANES), F32)
    return _pallas(
        body, name=name, grid=(n_pairs, len(DILATIONS)),
        in_specs=[qk_spec, qk_spec, v_spec, o_spec, o_spec, o_spec,
                  pl.BlockSpec((LANES, LANES), lambda hp, g: (0, 0))],
        out_specs=[qk_spec, qk_spec, qk_spec],
        out_shape=[jax.ShapeDtypeStruct((s, width), F32)] * 3,
        operands=[qn, kn, proj, dattn, attn, lse, bd],
        scratch_shapes=[pltpu.VMEM((s, LANES), BF16), pltpu.VMEM((spad, LANES), BF16),
                        pltpu.VMEM((spad, LANES), BF16), pltpu.VMEM((s, LANES), BF16),
                        f32buf, f32buf, f32buf, f32buf, f32buf, f32pad, f32pad,
                        f32buf, f32buf, f32buf, f32buf, pltpu.VMEM((QBLK, KWIN), F32),
                        pltpu.VMEM((2, 2 * QBLK, KWIN), BF16), pltpu.VMEM((2, 2 * QBLK, KWIN), BF16)],
        rider=rider)


CONV_PAD = 16


def _conv_fwd(proj, conv_w, conv_b, col0, name, rider=None):
    s = proj.shape[0]
    ch = conv_w.shape[1]
    nblk = ch // LANES
    a0 = col0 // LANES
    tr = 256
    shift = CONV_PAD - (CONV_WIDTH - 1) // 2

    def body(a_ref, b_ref, w_ref, bias_ref, u0_ref, uc_ref, pad):
        z = jnp.zeros((CONV_PAD, LANES), F32)
        pad[0:CONV_PAD, :] = z
        pad[s + CONV_PAD: s + 2 * CONV_PAD, :] = z

        def glu(rows):
            u0 = a_ref[rows, :] * jax.nn.sigmoid(b_ref[rows, :])
            u0_ref[rows, :] = u0
            pad[pl.ds(rows.start + CONV_PAD, rows.size), :] = u0

        _row_chunks(s, glu)
        for t in range(0, s, tr):
            acc = jnp.broadcast_to(bias_ref[...], (tr, LANES))
            for k in range(CONV_WIDTH):
                acc = acc + w_ref[k:k + 1, :] * pad[t + k + shift: t + k + shift + tr, :]
            uc_ref[t:t + tr, :] = acc

    return _pallas(
        body, name=name, grid=(nblk,),
        in_specs=[pl.BlockSpec((s, LANES), lambda c: (0, a0 + c)),
                  pl.BlockSpec((s, LANES), lambda c: (0, a0 + nblk + c)),
                  pl.BlockSpec((CONV_WIDTH, LANES), lambda c: (0, c)),
                  pl.BlockSpec((1, LANES), lambda c: (0, c))],
        out_specs=[pl.BlockSpec((s, LANES), lambda c: (0, c))] * 2,
        out_shape=[jax.ShapeDtypeStruct((s, ch), F32)] * 2, operands=[proj, proj, conv_w, conv_b],
        scratch_shapes=[pltpu.VMEM((s + 2 * CONV_PAD, LANES), F32)], rider=rider)


def _ln_silu_fwd(uc, ln_w, ln_b, name):
    s, ch = uc.shape
    tm = 256

    def body(u_ref, w_ref, b_ref, o_ref):
        u = u_ref[...]
        mu = jnp.mean(u, axis=-1, keepdims=True)
        xc = u - mu
        rstd = lax.rsqrt(jnp.mean(xc * xc, axis=-1, keepdims=True) + EPS)
        z = xc * rstd * w_ref[...] + b_ref[...]
        o_ref[...] = (z * jax.nn.sigmoid(z)).astype(BF16)

    row = pl.BlockSpec((tm, ch), lambda i: (i, 0))
    vec = pl.BlockSpec((1, ch), lambda i: (0, 0))
    return pl.pallas_call(
        body, name=name, grid=(s // tm,), in_specs=[row, vec, vec], out_specs=row,
        out_shape=jax.ShapeDtypeStruct((s, ch), BF16), compiler_params=_params(),
    )(uc, ln_w, ln_b)


def _ln_silu_bwd(du3, uc, ln_w, ln_b, name):
    s, ch = uc.shape
    tm = 256

    def body(d_ref, u_ref, w_ref, b_ref, du_ref, dw_ref, db_ref):
        u = u_ref[...]
        mu = jnp.mean(u, axis=-1, keepdims=True)
        xc = u - mu
        rstd = lax.rsqrt(jnp.mean(xc * xc, axis=-1, keepdims=True) + EPS)
        xhat = xc * rstd
        z = xhat * w_ref[...] + b_ref[...]
        sg = jax.nn.sigmoid(z)
        dz = d_ref[...] * (sg * (1.0 + z * (1.0 - sg)))
        dxh = dz * w_ref[...]
        du_ref[...] = rstd * (dxh - jnp.mean(dxh, axis=-1, keepdims=True)
                              - xhat * jnp.mean(dxh * xhat, axis=-1, keepdims=True))
        pw = jnp.sum(dz * xhat, axis=0, keepdims=True)
        pb = jnp.sum(dz, axis=0, keepdims=True)
        first = pl.program_id(0) == 0

        @pl.when(first)
        def _():
            dw_ref[...] = pw
            db_ref[...] = pb

        @pl.when(jnp.logical_not(first))
        def _():
            dw_ref[...] += pw
            db_ref[...] += pb

    row = pl.BlockSpec((tm, ch), lambda i: (i, 0))
    vec = pl.BlockSpec((1, ch), lambda i: (0, 0))
    return pl.pallas_call(
        body, name=name, grid=(s // tm,), in_specs=[row, row, vec, vec], out_specs=[row, vec, vec],
        out_shape=[jax.ShapeDtypeStruct((s, ch), F32), jax.ShapeDtypeStruct((1, ch), F32),
                   jax.ShapeDtypeStruct((1, ch), F32)],
        compiler_params=_params(),
    )(du3, uc, ln_w, ln_b)


def _conv_bwd(duc, u0, proj, conv_w, col0, name, rider=None):
    s = proj.shape[0]
    ch = conv_w.shape[1]
    nblk = ch // LANES
    a0 = col0 // LANES
    tr = 256
    half = (CONV_WIDTH - 1) // 2
    shift = CONV_PAD - half

    def body(duc_ref, u0_ref, a_ref, b_ref, w_ref, da_ref, db_ref, dw_ref, dbias_ref, pad_d, pad_u):
        z = jnp.zeros((CONV_PAD, LANES), F32)
        for buf in (pad_d, pad_u):
            buf[0:CONV_PAD, :] = z
            buf[s + CONV_PAD: s + 2 * CONV_PAD, :] = z

        def fill(rows):
            dst = pl.ds(rows.start + CONV_PAD, rows.size)
            pad_d[dst, :] = duc_ref[rows, :]
            pad_u[dst, :] = u0_ref[rows, :]

        _row_chunks(s, fill)
        dw_acc = [jnp.zeros((8, LANES), F32) for _ in range(CONV_WIDTH)]
        dbias_acc = jnp.zeros((8, LANES), F32)
        for t in range(0, s, tr):
            d_t = duc_ref[t:t + tr, :]
            dbias_acc = dbias_acc + jnp.sum(d_t.reshape(tr // 8, 8, LANES), axis=0)
            du0 = jnp.zeros((tr, LANES), F32)
            for k in range(CONV_WIDTH):
                du0 = du0 + w_ref[k:k + 1, :] * pad_d[t - k + half + CONV_PAD: t - k + half + CONV_PAD + tr, :]
                prod = d_t * pad_u[t + k + shift: t + k + shift + tr, :]
                dw_acc[k] = dw_acc[k] + jnp.sum(prod.reshape(tr // 8, 8, LANES), axis=0)
            av = a_ref[t:t + tr, :]
            sg = jax.nn.sigmoid(b_ref[t:t + tr, :])
            da_ref[t:t + tr, :] = (du0 * sg).astype(BF16)
            db_ref[t:t + tr, :] = (du0 * av * sg * (1.0 - sg)).astype(BF16)
        for k in range(CONV_WIDTH):
            dw_ref[k:k + 1, :] = jnp.sum(dw_acc[k], axis=0, keepdims=True)
        dbias_ref[...] = jnp.sum(dbias_acc, axis=0, keepdims=True)

    col = lambda off: pl.BlockSpec((s, LANES), lambda c: (0, off + c))
    return _pallas(
        body, name=name, grid=(nblk,),
        in_specs=[col(0), col(0), col(a0), col(a0 + nblk),
                  pl.BlockSpec((CONV_WIDTH, LANES), lambda c: (0, c))],
        out_specs=[col(0), col(0), pl.BlockSpec((CONV_WIDTH, LANES), lambda c: (0, c)),
                   pl.BlockSpec((1, LANES), lambda c: (0, c))],
        out_shape=[jax.ShapeDtypeStruct((s, ch), BF16)] * 2
        + [jax.ShapeDtypeStruct((CONV_WIDTH, ch), F32), jax.ShapeDtypeStruct((1, ch), F32)],
        operands=[duc, u0, proj, proj, conv_w],
        scratch_shapes=[pltpu.VMEM((s + 2 * CONV_PAD, LANES), F32)] * 2, rider=rider)


GATE_BLK = 512


def _gate_fwd(proj, bg, y_a, y_b, col0, name):
    s, d = y_a.shape
    tm = 256
    g0 = col0 // GATE_BLK
    nb = d // GATE_BLK

    def body(ga_ref, gb_ref, ba_ref, bb_ref, ya_ref, yb_ref, o_ref):
        g_a = jax.nn.sigmoid(ga_ref[...] + ba_ref[...])
        g_b = jax.nn.sigmoid(gb_ref[...] + bb_ref[...])
        o_ref[...] = (g_a * ya_ref[...] + g_b * yb_ref[...]).astype(BF16)

    act = pl.BlockSpec((tm, GATE_BLK), lambda i, j: (i, j))
    return pl.pallas_call(
        body, name=name, grid=(s // tm, nb),
        in_specs=[pl.BlockSpec((tm, GATE_BLK), lambda i, j: (i, g0 + j)),
                  pl.BlockSpec((tm, GATE_BLK), lambda i, j: (i, g0 + nb + j)),
                  pl.BlockSpec((None, 1, GATE_BLK), lambda i, j: (0, 0, j)),
                  pl.BlockSpec((None, 1, GATE_BLK), lambda i, j: (1, 0, j)), act, act],
        out_specs=act, out_shape=jax.ShapeDtypeStruct((s, d), BF16), compiler_params=_params(),
    )(proj, proj, bg, bg, y_a, y_b)


def _out_proj_bwd_gates(dx1, w_out, proj, bg, y_a, y_b, col0, name, after=()):
    s, d = y_a.shape
    tm = 256
    half = d // 2
    assert col0 % half == 0
    c0 = col0 // half
    nt_dims = (((1,), (1,)), ((), ()))

    def body(dx_ref, w_ref, a0_ref, a1_ref, b0_ref, b1_ref, bias_ref, ya_ref, yb_ref,
             dgl_ref, dya_ref, dyb_ref, db_ref):
        dm = lax.dot_general(dx_ref[...], w_ref[...], nt_dims, preferred_element_type=F32)
        parts = []
        for br, (lo_ref, hi_ref, y_ref, dy_ref) in enumerate(((a0_ref, a1_ref, ya_ref, dya_ref),
                                                              (b0_ref, b1_ref, yb_ref, dyb_ref))):
            logits = jnp.concatenate([lo_ref[...], hi_ref[...]], axis=1)
            gate = jax.nn.sigmoid(logits + bias_ref[br])
            dy_ref[...] = (dm * gate).astype(BF16)
            dgl = dm * y_ref[...] * gate * (1.0 - gate)
            dgl_ref[:, br * d:(br + 1) * d] = dgl.astype(BF16)
            parts.append(jnp.sum(dgl, axis=0, keepdims=True))
        part = jnp.concatenate(parts, axis=0)
        first = pl.program_id(0) == 0

        @pl.when(first)
        def _():
            db_ref[...] = part

        @pl.when(jnp.logical_not(first))
        def _():
            db_ref[...] += part

    row = pl.BlockSpec((tm, d), lambda i: (i, 0))
    logit_blk = lambda k: pl.BlockSpec((tm, half), functools.partial(lambda i, k: (i, c0 + k), k=k))
    return _pallas(
        body, name=name, grid=(s // tm,),
        in_specs=[row, pl.BlockSpec((d, d), lambda i: (0, 0)), logit_blk(0), logit_blk(1), logit_blk(2),
                  logit_blk(3), pl.BlockSpec((2, 1, d), lambda i: (0, 0, 0)), row, row],
        out_specs=[pl.BlockSpec((tm, 2 * d), lambda i: (i, 0)), row, row, pl.BlockSpec((2, d), lambda i: (0, 0))],
        out_shape=[jax.ShapeDtypeStruct((s, 2 * d), BF16), jax.ShapeDtypeStruct((s, d), BF16),
                   jax.ShapeDtypeStruct((s, d), BF16), jax.ShapeDtypeStruct((2, d), F32)],
        operands=[dx1, w_out, proj, proj, proj, proj, bg, y_a, y_b], after=after)


def _ffn_in_swiglu(h2, w_blocked, name):
    s, k = h2.shape
    nblk, _, tn = w_blocked.shape
    ff = nblk // 2 * tn
    tm = 512

    def body(a_ref, wg_ref, wu_ref, g_ref, u_ref, act_ref):
        a = a_ref[...]
        gt = jnp.dot(a, wg_ref[...], preferred_element_type=F32)
        up = jnp.dot(a, wu_ref[...], preferred_element_type=F32)
        g_ref[...] = gt
        u_ref[...] = up
        act_ref[...] = (gt * jax.nn.sigmoid(gt) * up).astype(BF16)

    out = pl.BlockSpec((tm, tn), lambda j, i: (i, j))
    return pl.pallas_call(
        body, name=name, grid=(nblk // 2, s // tm),
        in_specs=[pl.BlockSpec((tm, k), lambda j, i: (i, 0)),
                  pl.BlockSpec((None, k, tn), lambda j, i: (j, 0, 0)),
                  pl.BlockSpec((None, k, tn), lambda j, i: (nblk // 2 + j, 0, 0))],
        out_specs=[out, out, out],
        out_shape=[jax.ShapeDtypeStruct((s, ff), F32), jax.ShapeDtypeStruct((s, ff), F32),
                   jax.ShapeDtypeStruct((s, ff), BF16)],
        compiler_params=_params(),
    )(h2, w_blocked, w_blocked)


def _ffn_out_bwd_swiglu(dy, w_ffn_out, gate, up, name, rider=None):
    s, d = dy.shape
    ff = gate.shape[1]
    tm = 256
    nt_dims = (((1,), (1,)), ((), ()))

    def body(dy_ref, w_ref, g_ref, u_ref, o_ref):
        dv = lax.dot_general(dy_ref[...], w_ref[...], nt_dims, preferred_element_type=F32)
        gt = g_ref[...]
        sg = jax.nn.sigmoid(gt)
        o_ref[:, 0:ff] = (dv * u_ref[...] * (sg * (1.0 + gt * (1.0 - sg)))).astype(BF16)
        o_ref[:, ff:2 * ff] = (dv * gt * sg).astype(BF16)

    row = pl.BlockSpec((tm, ff), lambda i: (i, 0))
    return _pallas(
        body, name=name, grid=(s // tm,),
        in_specs=[pl.BlockSpec((tm, d), lambda i: (i, 0)), pl.BlockSpec((ff, d), lambda i: (0, 0)), row, row],
        out_specs=pl.BlockSpec((tm, 2 * ff), lambda i: (i, 0)),
        out_shape=jax.ShapeDtypeStruct((s, 2 * ff), BF16), operands=[dy, w_ffn_out, gate, up], rider=rider)


def _out_proj_rmsnorm(mixed, w_out, x, norm_w, name):
    s, k = mixed.shape
    d = w_out.shape[1]
    tm = 512

    def body(a_ref, w_ref, x_ref, nw_ref, x1_ref, h2_ref):
        x1 = x_ref[...] + jnp.dot(a_ref[...], w_ref[...], preferred_element_type=F32)
        x1_ref[...] = x1
        rstd = lax.rsqrt(jnp.mean(x1 * x1, axis=-1, keepdims=True) + EPS)
        h2_ref[...] = (x1 * rstd * nw_ref[...]).astype(BF16)

    row = pl.BlockSpec((tm, d), lambda i: (i, 0))
    return pl.pallas_call(
        body, name=name, grid=(s // tm,),
        in_specs=[pl.BlockSpec((tm, k), lambda i: (i, 0)), pl.BlockSpec((k, d), lambda i: (0, 0)), row,
                  pl.BlockSpec((1, d), lambda i: (0, 0))],
        out_specs=[row, row],
        out_shape=[jax.ShapeDtypeStruct((s, d), F32), jax.ShapeDtypeStruct((s, d), BF16)],
        compiler_params=_params(),
    )(mixed, w_out, x, norm_w)


def _ffn_out_loss(act, w_ffn_out, x1, target, name):
    s, k = act.shape
    d = w_ffn_out.shape[1]
    tm = 512

    def body(a_ref, w_ref, x1_ref, t_ref, dy_ref, dyb_ref, loss_ref, acc):
        y = x1_ref[...] + jnp.dot(a_ref[...], w_ref[...], preferred_element_type=F32)
        diff = y - t_ref[...]
        dy = diff * (1.0 / d)
        dy_ref[...] = dy
        dyb_ref[...] = dy.astype(BF16)
        part = jnp.sum((diff * diff).reshape(tm // 8, 8, d), axis=0)
        i = pl.program_id(0)

        @pl.when(i == 0)
        def _():
            acc[...] = part

        @pl.when(i > 0)
        def _():
            acc[...] += part

        @pl.when(i == pl.num_programs(0) - 1)
        def _():
            loss_ref[...] = (0.5 / d) * jnp.sum(jnp.sum(acc[...], axis=1, keepdims=True), axis=0, keepdims=True)

    row = pl.BlockSpec((tm, d), lambda i: (i, 0))
    return pl.pallas_call(
        body, name=name, grid=(s // tm,),
        in_specs=[pl.BlockSpec((tm, k), lambda i: (i, 0)), pl.BlockSpec((k, d), lambda i: (0, 0)), row, row],
        out_specs=[row, row, pl.BlockSpec((1, 1), lambda i: (0, 0))],
        out_shape=[jax.ShapeDtypeStruct((s, d), F32), jax.ShapeDtypeStruct((s, d), BF16),
                   jax.ShapeDtypeStruct((1, 1), F32)],
        scratch_shapes=[pltpu.VMEM((8, d), F32)], compiler_params=_params(),
    )(act, w_ffn_out, x1, target)


LATE_GATHER = ("w_o_attn", "w_pw_conv", "w_out", "w_ffn_in", "w_ffn_out")
EARLY_REDUCE = LATE_GATHER


def _blocks_by_half(g):
    if g.ndim == 2:
        g = g.reshape(N_CHIPS, g.shape[0] // N_CHIPS, g.shape[1])
    return g.reshape(N_CHIPS, 2, g.shape[1] // 2, g.shape[2])


def _forward_backward(x, pos_col, target, wts, late_bufs, pos_arr):
    wts = dict(wts)
    consts = _rope_consts()
    bd = consts[3]
    qw2 = jnp.tile(wts["q_norm_w"], (1, LANES // HEAD_DIM))
    kw2 = jnp.tile(wts["k_norm_w"], (1, LANES // HEAD_DIM))
    qkv_w = 3 * N_SLOT_HEADS * HEAD_DIM
    conv_col0 = 3 * qkv_w
    ch = wts["conv_w"].shape[1]
    gate_col0 = conv_col0 + 2 * ch

    h = _rmsnorm_fwd(x, wts["norm1_w"], "rms1_fwd")
    gathering, started = _split_start(_gather_ici_rider(late_bufs, []), "late_gather_start", after=[wts["w_in"]])
    proj = _matmul(h, wts["w_in"], mode="nn", tm=512, tn=1920, tk=1024, out_dtype=F32, name="mm_proj",
                   b_blocked=True, cols_outer=True, after=[started])
    qn, kn = _qk_fwd(proj, pos_col, qw2, kw2, consts, "qk_fwd")
    attn, lse, attn_b = _attn_fwd(qn, kn, proj, "attn_fwd")
    late_bufs, _ = _split_wait(gathering, after=[attn_b], name="late_gather_wait")
    (u0, uc), late_bufs = _conv_fwd(proj, wts["conv_w"], wts["conv_b"], conv_col0, "conv_fwd",
                                    rider=_gather_forward_rider(late_bufs))
    for n, buf in zip(LATE_GATHER, late_bufs):
        full = buf.reshape(N_CHIPS, -1, buf.shape[3])
        wts[n] = full.reshape(-1, full.shape[2]) if n in ROW_SHARDED else full
    y_a = _matmul(attn_b, wts["w_o_attn"], mode="nn", tm=1024, tn=256, tk=512, out_dtype=F32, name="mm_ya",
                  b_blocked=True)
    u3 = _ln_silu_fwd(uc, wts["conv_ln_w"], wts["conv_ln_b"], "ln_fwd")
    y_b = _matmul(u3, wts["w_pw_conv"], mode="nn", tm=1024, tn=256, tk=512, out_dtype=F32, name="mm_yb",
                  b_blocked=True)
    mixed = _gate_fwd(proj, wts["b_gate"], y_a, y_b, gate_col0, "gate_fwd")
    x1, h2 = _out_proj_rmsnorm(mixed, wts["w_out"], x, wts["norm2_w"], "mm_x1_rms2")
    gate, up, act = _ffn_in_swiglu(h2, wts["w_ffn_in"], "mm_gu_swiglu")
    dy, dy_b16, loss = _ffn_out_loss(act, wts["w_ffn_out"], x1, target, "mm_x2_loss")

    g = {}
    by_chip = {}

    def pair_add(n, blocks, received):
        return _add_own_half(blocks, received, pos_arr, f"grads_pair_add_{n}")

    g_ffn_out = _blocks_by_half(
        _matmul(act, dy_b16, mode="tn", tm=1408, tn=1024, tk=2048, out_dtype=F32, name="mm_dwffnout"))
    dgu, (received,) = _ffn_out_bwd_swiglu(dy_b16, wts["w_ffn_out"], gate, up, "mm_dact_swiglu_bwd",
                                           rider=_pair_exchange_rider([g_ffn_out], halved=True))
    to_send, own = pair_add("w_ffn_out", g_ffn_out, received)
    dh2, (by_chip["w_ffn_out"],) = _matmul(
        dgu, wts["w_ffn_in"], mode="nt", tm=1024, tn=1024, tk=1408, out_dtype=F32, name="mm_dh2", b_blocked=True,
        rider=_chip_exchange_rider([to_send], [own]))
    g_ffn_in = _blocks_by_half(_matmul(h2, dgu, mode="tn", tm=512, tn=1408, tk=2048, out_dtype=F32,
                                       name="mm_dwffnin", out_blocked=N_CHIPS, cols_outer=True))
    exchanging, started = _split_start(_pair_exchange_rider([g_ffn_in], halved=True), "grads_ffn_in_pair_start")
    dx1, dx1_b16, g["norm2_w"] = _rmsnorm_bwd(dh2, x1, wts["norm2_w"], dy, "rms2_bwd")
    g["w_out"] = _matmul(mixed, dx1_b16, mode="tn", tm=512, tn=1024, tk=2048, out_dtype=F32, name="mm_dwout",
                         after=[started])
    dgl, dy_a, dy_b, g["b_gate"] = _out_proj_bwd_gates(dx1_b16, wts["w_out"], proj, wts["b_gate"], y_a, y_b,
                                                       gate_col0, "mm_dmixed_gate_bwd")
    (received,), (g_ffn_in,) = _split_wait(exchanging, after=[dgl], name="grads_ffn_in_pair_wait")
    ffn_in_to_send, ffn_in_own = pair_add("w_ffn_in", g_ffn_in, received)
    dattn = _matmul(dy_a, wts["w_o_attn"], mode="nt", tm=1024, tn=512, tk=256, out_dtype=F32, name="mm_dattn",
                    b_blocked=True)
    g["w_o_attn"] = _matmul(attn_b, dy_a, mode="tn", tm=512, tn=256, tk=2048, out_dtype=F32, name="mm_dwo",
                            out_blocked=N_CHIPS)
    du3 = _matmul(dy_b, wts["w_pw_conv"], mode="nt", tm=1024, tn=512, tk=256, out_dtype=F32, name="mm_du3",
                  b_blocked=True)
    g["w_pw_conv"] = _matmul(u3, dy_b, mode="tn", tm=512, tn=256, tk=2048, out_dtype=F32, name="mm_dwpw",
                             out_blocked=N_CHIPS)
    duc, g["conv_ln_w"], g["conv_ln_b"] = _ln_silu_bwd(du3, uc, wts["conv_ln_w"], wts["conv_ln_b"], "ln_bwd")

    small3 = ("w_out", "w_o_attn", "w_pw_conv")
    g_small3 = [_blocks_by_half(g.pop(n)) for n in small3]
    (da, db, g["conv_w"], g["conv_b"]), received = _conv_bwd(
        duc, u0, proj, wts["conv_w"], conv_col0, "conv_bwd", rider=_pair_exchange_rider(g_small3, halved=True))
    sums3 = [pair_add(n, gb, rv) for n, gb, rv in zip(small3, g_small3, received)]
    (dqn, dkn, dv), (by_chip["w_ffn_in"],) = _attn_bwd(
        qn, kn, proj, dattn, attn, lse, bd, "attn_bwd",
        rider=_chip_exchange_rider([ffn_in_to_send], [ffn_in_own]))
    (dproj, dqw, dkw), exchanged3 = _qk_bwd(
        dqn, dkn, dv, da, db, dgl, proj, pos_col, qw2, kw2, consts, "qk_bwd",
        rider=_chip_exchange_rider([s[0] for s in sums3], [s[1] for s in sums3]))
    by_chip.update(zip(small3, exchanged3))
    halves = [_sum_chips(by_chip[n], pos_arr, f"grads_chip_sum_{n}") for n in EARLY_REDUCE]
    g["q_norm_w"] = dqw[:, :HEAD_DIM]
    g["k_norm_w"] = dkw[:, :HEAD_DIM]

    c = pos_arr[0]
    rh = h.shape[1] // 2
    h_sibling = lax.dynamic_slice_in_dim(h, (1 - c) * rh, rh, axis=1)
    h_own = lax.dynamic_slice_in_dim(h, c * rh, rh, axis=1)
    g_sibling, shards = _matmul(h_sibling, dproj, mode="tn", tm=rh, tn=1920, tk=2048, out_dtype=F32,
                                name="mm_dwin_sibling", out_blocked=N_CHIPS, rider=_pair_gather_rider(halves))
    reduced = dict(zip(EARLY_REDUCE, shards))
    exchanging, started = _split_start(_pair_exchange_rider([g_sibling], halved=False), "grads_w_in_pair_start")
    g_own = _matmul(h_own, dproj, mode="tn", tm=rh, tn=1920, tk=2048, out_dtype=F32, name="mm_dwin_own",
                    out_blocked=N_CHIPS, after=[started])
    (from_sibling,), _ = _split_wait(exchanging, after=[g_own], name="grads_w_in_pair_wait")
    to_send, own = _add_own_half(g_own, from_sibling, pos_arr, "grads_pair_add_w_in")
    in_flight, started = _split_start(_chip_exchange_rider([to_send], [own]), "grads_w_in_exchange_start")
    dh = _matmul(dproj, wts["w_in"], mode="nt", tm=1024, tn=1024, tk=1920, out_dtype=F32, name="mm_dh",
                 b_blocked=True, after=[started])
    grad_x, _, g["norm1_w"] = _rmsnorm_bwd(dh, x, wts["norm1_w"], dx1, "rms1_bwd")
    return loss, grad_x, g, reduced, (in_flight, started)


def _mesh_pos():
    return lax.axis_index("x"), lax.axis_index("y"), lax.axis_index("c")


def _other_chips(x, y):
    return [(1 - x, y), (x, 1 - y), (1 - x, 1 - y)]


def _cast_into_slot(shard, chip_arr, dtype, name):
    r, c = shard.shape
    tr = r // 2 if r % 32 == 0 else r

    def body(chip_ref, s_ref, o_ref):
        del chip_ref
        o_ref[...] = s_ref[...].astype(dtype)

    return pl.pallas_call(
        body, name=name,
        grid_spec=pltpu.PrefetchScalarGridSpec(
            num_scalar_prefetch=1, grid=(r // tr,),
            in_specs=[pl.BlockSpec((tr, c), lambda i, chip_ref: (i, 0))],
            out_specs=pl.BlockSpec((None, tr, c), lambda i, chip_ref: (chip_ref[0], i, 0))),
        out_shape=jax.ShapeDtypeStruct((N_CHIPS, r, c), dtype), compiler_params=_params(),
    )(chip_arr, shard)


GATHER_CHUNKS = 4


def _gather_both_legs_rider(big, small):
    nb = len(big)
    n = nb + len(small)
    nch = GATHER_CHUNKS

    def part(bufs, a, slot, half, ch):
        if a >= nb:
            return bufs[a].at[slot]
        rows = bufs[a].shape[2] // nch
        return bufs[a].at[slot, half, pl.ds(ch * rows, rows)]

    def pieces():
        return [(a, ch, k) for ch in range(nch) for a in range(n) for k in range(3) if a < nb or ch == 0]

    def ici(bufs, sems, a, ch, k, slot_of_src):
        x, y, c = _mesh_pos()
        px, py = _other_chips(x, y)[k]
        slot = 2 * x + y if slot_of_src == "mine" else 2 * px + py
        return pltpu.make_async_remote_copy(
            src_ref=part(bufs, a, slot, c, ch), dst_ref=part(bufs, a, slot, c, ch), send_sem=sems[0].at[a, ch, k],
            recv_sem=sems[1].at[a, ch, k], device_id=(px, py, c), device_id_type=MESH)

    def forward(bufs, sems, a, ch, k, half):
        x, y, c = _mesh_pos()
        px, py = _other_chips(x, y)[k]
        h = c if half == "mine" else 1 - c
        return pltpu.make_async_remote_copy(
            src_ref=part(bufs, a, 2 * px + py, h, ch), dst_ref=part(bufs, a, 2 * px + py, h, ch),
            send_sem=sems[2].at[a, ch, k], recv_sem=sems[3].at[a, ch, k], device_id=(x, y, 1 - c),
            device_id_type=MESH)

    def start(r_in, bufs, sems):
        for a, ch, k in pieces():
            ici(bufs, sems, a, ch, k, "mine").start()

    def wait(r_in, bufs, sems):
        for a, ch, k in pieces():
            ici(bufs, sems, a, ch, k, "theirs").wait_recv()
            if a < nb:
                forward(bufs, sems, a, ch, k, "mine").start()
        for a, ch, k in pieces():
            if a < nb:
                forward(bufs, sems, a, ch, k, "theirs").wait_recv()
        for a, ch, k in pieces():
            ici(bufs, sems, a, ch, k, "mine").wait_send()
            if a < nb:
                forward(bufs, sems, a, ch, k, "mine").wait_send()

    ops = list(big) + list(small)
    return _Rider(ops, [jax.ShapeDtypeStruct(o.shape, o.dtype) for o in ops], {i: i for i in range(n)},
                  [pltpu.SemaphoreType.DMA((n, nch, 3)), pltpu.SemaphoreType.DMA((n, nch, 3)),
                   pltpu.SemaphoreType.DMA((nb, nch, 3)), pltpu.SemaphoreType.DMA((nb, nch, 3))], start, wait)


def _comm_call(rider, name):
    def body():
        pass

    return _pallas(body, name=name, grid=(1,), in_specs=[], out_specs=[], out_shape=[], operands=[],
                   rider=rider)[1]


def _gather_ici_rider(big, small):
    nb = len(big)
    n = nb + len(small)

    def copies(bufs, sems):
        x, y, c = _mesh_pos()
        me = 2 * x + y
        part = lambda a, slot: bufs[a].at[slot, c] if a < nb else bufs[a].at[slot]
        out = []
        for a in range(n):
            for k, (px, py) in enumerate(_other_chips(x, y)):
                send = functools.partial(
                    pltpu.make_async_remote_copy,
                    src_ref=part(a, me), dst_ref=part(a, me), send_sem=sems[0].at[a, k],
                    recv_sem=sems[1].at[a, k], device_id=(px, py, c), device_id_type=MESH)
                recv = functools.partial(
                    pltpu.make_async_remote_copy,
                    src_ref=part(a, 2 * px + py), dst_ref=part(a, 2 * px + py), send_sem=sems[0].at[a, k],
                    recv_sem=sems[1].at[a, k], device_id=(px, py, c), device_id_type=MESH)
                out.append((send, recv))
        return out

    def start(r_in, r_out, sems):
        for send, _ in copies(r_out, sems):
            send().start()

    def wait(r_in, r_out, sems):
        cps = copies(r_out, sems)
        for _, recv in cps:
            recv().wait_recv()
        for send, _ in cps:
            send().wait_send()

    ops = list(big) + list(small)
    return _Rider(ops, [jax.ShapeDtypeStruct(o.shape, o.dtype) for o in ops], {i: i for i in range(n)},
                  [pltpu.SemaphoreType.DMA((n, 3)), pltpu.SemaphoreType.DMA((n, 3))], start, wait)


def _gather_forward_rider(big):
    n = len(big)

    def copies(bufs, sems):
        x, y, c = _mesh_pos()
        out = []
        for a in range(n):
            for k, (px, py) in enumerate(_other_chips(x, y)):
                slot = 2 * px + py
                send = functools.partial(
                    pltpu.make_async_remote_copy,
                    src_ref=bufs[a].at[slot, c], dst_ref=bufs[a].at[slot, c], send_sem=sems[0].at[a, k],
                    recv_sem=sems[1].at[a, k], device_id=(x, y, 1 - c), device_id_type=MESH)
                recv = functools.partial(
                    pltpu.make_async_remote_copy,
                    src_ref=bufs[a].at[slot, 1 - c], dst_ref=bufs[a].at[slot, 1 - c], send_sem=sems[0].at[a, k],
                    recv_sem=sems[1].at[a, k], device_id=(x, y, 1 - c), device_id_type=MESH)
                out.append((send, recv))
        return out

    def start(r_in, r_out, sems):
        for send, _ in copies(r_out, sems):
            send().start()

    def wait(r_in, r_out, sems):
        cps = copies(r_out, sems)
        for _, recv in cps:
            recv().wait_recv()
        for send, _ in cps:
            send().wait_send()

    return _Rider(big, [jax.ShapeDtypeStruct(o.shape, o.dtype) for o in big], {i: i for i in range(n)},
                  [pltpu.SemaphoreType.DMA((n, 3)), pltpu.SemaphoreType.DMA((n, 3))], start, wait)


def _pair_exchange_rider(gs, halved):
    n = len(gs)

    def copies(r_in, r_out, sems):
        x, y, c = _mesh_pos()
        return [pltpu.make_async_remote_copy(
            src_ref=r_in[a].at[:, 1 - c] if halved else r_in[a], dst_ref=r_out[a], send_sem=sems[0].at[a],
            recv_sem=sems[1].at[a], device_id=(x, y, 1 - c), device_id_type=MESH) for a in range(n)]

    def start(r_in, r_out, sems):
        for cp in copies(r_in, r_out, sems):
            cp.start()

    def wait(r_in, r_out, sems):
        for cp in copies(r_in, r_out, sems):
            cp.wait()

    return _Rider(gs, [jax.ShapeDtypeStruct((g.shape[0],) + g.shape[-2:], g.dtype) for g in gs], {},
                  [pltpu.SemaphoreType.DMA((n,)), pltpu.SemaphoreType.DMA((n,))], start, wait)


def _chip_exchange_rider(to_send, by_chip, row_range=None):
    n = len(to_send)

    def copies(r_in, r_out, sems):
        x, y, c = _mesh_pos()
        me = 2 * x + y
        rows = (lambda ref: ref) if row_range is None else (lambda ref: ref.at[pl.ds(*row_range)])
        out = []
        for a in range(n):
            for k, (px, py) in enumerate(_other_chips(x, y)):
                send = functools.partial(
                    pltpu.make_async_remote_copy,
                    src_ref=rows(r_in[a].at[2 * px + py]), dst_ref=rows(r_out[a].at[me]),
                    send_sem=sems[0].at[a, k], recv_sem=sems[1].at[a, k], device_id=(px, py, c),
                    device_id_type=MESH)
                recv = functools.partial(
                    pltpu.make_async_remote_copy,
                    src_ref=rows(r_in[a].at[me]), dst_ref=rows(r_out[a].at[2 * px + py]),
                    send_sem=sems[0].at[a, k], recv_sem=sems[1].at[a, k], device_id=(px, py, c),
                    device_id_type=MESH)
                out.append((send, recv))
        return out

    def start(r_in, r_out, sems):
        for send, _ in copies(r_in, r_out, sems):
            send().start()

    def wait(r_in, r_out, sems):
        cps = copies(r_in, r_out, sems)
        for _, recv in cps:
            recv().wait_recv()
        for send, _ in cps:
            send().wait_send()

    return _Rider(list(to_send) + list(by_chip), [jax.ShapeDtypeStruct(b.shape, b.dtype) for b in by_chip],
                  {n + i: i for i in range(n)},
                  [pltpu.SemaphoreType.DMA((n, 3)), pltpu.SemaphoreType.DMA((n, 3))], start, wait)


HBM = pl.BlockSpec(memory_space=pltpu.HBM)
SEM = pl.BlockSpec(memory_space=pltpu.SEMAPHORE)


_IN_FLIGHT = pltpu.CompilerParams(has_side_effects=pltpu.SideEffectType.DATAFLOW_SIDE_EFFECTING)


class _FlatSems:
    def __init__(self, ref, shape):
        self.ref, self.shape = ref, shape

    @property
    def at(self):
        return self

    def __getitem__(self, idx):
        idx = idx if isinstance(idx, tuple) else (idx,)
        flat = 0
        for i, n in zip(idx, self.shape):
            flat = flat * n + i
        return self.ref.at[flat]


def _flat_sem_types(rider):
    return tuple(pltpu.SemaphoreType.DMA((int(np.prod(s.shape)),)) for s in rider.scratch)


def _as_rider_sems(rider, refs):
    return [_FlatSems(r, s.shape) for r, s in zip(refs, rider.scratch)]


def _split_start(rider, name, after=()):
    n_in, n_out, n_sem = len(rider.operands), len(rider.out_shapes), len(rider.scratch)
    n_after = len(after)
    fresh = [j for j in range(n_out) if j not in rider.aliases.values()]
    by_out = {j: i for i, j in rider.aliases.items()}

    def body(*refs):
        r_in = refs[:n_in]
        refs = refs[n_in + n_after:]
        sems = refs[:n_sem]
        thru = refs[n_sem:n_sem + n_in]
        fresh_refs = refs[n_sem + n_in:n_sem + n_in + len(fresh)]
        token = refs[-1]
        r_out = [thru[by_out[j]] if j in by_out else fresh_refs[fresh.index(j)] for j in range(n_out)]
        rider.start(r_in, r_out, _as_rider_sems(rider, sems))
        token[...] = jnp.zeros_like(token)

    res = pl.pallas_call(
        body, name=name,
        out_shape=_flat_sem_types(rider) + tuple(pltpu.HBM(o.shape, o.dtype) for o in rider.operands)
        + tuple(pltpu.HBM(rider.out_shapes[j].shape, rider.out_shapes[j].dtype) for j in fresh)
        + (jax.ShapeDtypeStruct((8, LANES), F32),),
        in_specs=(HBM,) * n_in + (ANY,) * n_after,
        out_specs=(SEM,) * n_sem + (HBM,) * (n_in + len(fresh)) + (pl.BlockSpec(memory_space=pltpu.VMEM),),
        input_output_aliases={i: n_sem + i for i in range(n_in)}, compiler_params=_IN_FLIGHT,
    )(*[pltpu.with_memory_space_constraint(o, pltpu.HBM) for o in rider.operands], *after)
    return (rider, res[:n_sem], res[n_sem:n_sem + n_in], res[n_sem + n_in:-1]), res[-1]


def _split_wait(handles, after, name):
    rider, sems, thru, fresh_arrays = handles
    n_in, n_out, n_sem = len(rider.operands), len(rider.out_shapes), len(rider.scratch)
    fresh = [j for j in range(n_out) if j not in rider.aliases.values()]
    by_out = {j: i for i, j in rider.aliases.items()}
    n_data = n_in + len(fresh)

    def body(*refs):
        r_in = refs[:n_in]
        fresh_refs = refs[n_in:n_data]
        sem_refs = refs[n_data:n_data + n_sem]
        r_out = [r_in[by_out[j]] if j in by_out else fresh_refs[fresh.index(j)] for j in range(n_out)]
        rider.wait(r_in, r_out, _as_rider_sems(rider, sem_refs))

    data = list(thru) + list(fresh_arrays)
    res = pl.pallas_call(
        body, name=name, out_shape=tuple(pltpu.HBM(d.shape, d.dtype) for d in data),
        in_specs=(HBM,) * n_data + (SEM,) * n_sem + (ANY,) * len(after), out_specs=(HBM,) * n_data,
        input_output_aliases={i: i for i in range(n_data)}, compiler_params=_IN_FLIGHT,
    )(*data, *sems, *after)
    return [res[by_out[j]] if j in by_out else res[n_in + fresh.index(j)] for j in range(n_out)], res[:n_in]


def _pair_gather_rider(bufs):
    n = len(bufs)

    def copies(r_out, sems):
        x, y, c = _mesh_pos()
        out = []
        for a in range(n):
            send = functools.partial(
                    pltpu.make_async_remote_copy,
                src_ref=r_out[a].at[c], dst_ref=r_out[a].at[c], send_sem=sems[0].at[a],
                recv_sem=sems[1].at[a], device_id=(x, y, 1 - c), device_id_type=MESH)
            recv = functools.partial(
                    pltpu.make_async_remote_copy,
                src_ref=r_out[a].at[1 - c], dst_ref=r_out[a].at[1 - c], send_sem=sems[0].at[a],
                recv_sem=sems[1].at[a], device_id=(x, y, 1 - c), device_id_type=MESH)
            out.append((send, recv))
        return out

    def start(r_in, r_out, sems):
        for send, _ in copies(r_out, sems):
            send().start()

    def wait(r_in, r_out, sems):
        cps = copies(r_out, sems)
        for _, recv in cps:
            recv().wait_recv()
        for send, _ in cps:
            send().wait_send()

    return _Rider(bufs, [jax.ShapeDtypeStruct(b.shape, b.dtype) for b in bufs], {i: i for i in range(n)},
                  [pltpu.SemaphoreType.DMA((n,)), pltpu.SemaphoreType.DMA((n,))], start, wait)


def _add_own_half(g, recv, pos_arr, name):
    nb, rh, cols = g.shape[0], g.shape[-2], g.shape[-1]

    def body(pos_ref, g_ref, r_ref, send_ref, own_ref):
        s = (g_ref[...] + r_ref[...]).astype(BF16)
        send_ref[...] = s

        @pl.when(pl.program_id(0) == pos_ref[1])
        def _():
            own_ref[...] = s

    blk = pl.BlockSpec((None, rh, cols), lambda j, pos_ref: (j, 0, 0))
    g_spec = blk if g.ndim == 3 else pl.BlockSpec((None, None, rh, cols),
                                                   lambda j, pos_ref: (j, pos_ref[0], 0, 0))
    shape = jax.ShapeDtypeStruct((nb, rh, cols), BF16)
    return pl.pallas_call(
        body, name=name,
        grid_spec=pltpu.PrefetchScalarGridSpec(
            num_scalar_prefetch=1, grid=(nb,), in_specs=[g_spec, blk],
            out_specs=[blk, pl.BlockSpec((None, rh, cols), lambda j, pos_ref: (pos_ref[1], 0, 0))]),
        out_shape=[shape, shape], compiler_params=_params(),
    )(pos_arr, g, recv)


def _sum_chips(gath, pos_arr, name):
    nb, rh, cols = gath.shape

    def body(pos_ref, a_ref, b_ref, c_ref, d_ref, o_ref):
        del pos_ref
        o_ref[...] = ((a_ref[...].astype(F32) + b_ref[...].astype(F32)) + c_ref[...].astype(F32)) \
            + d_ref[...].astype(F32)

    tr = rh // 2 if (rh // 2) % 16 == 0 else rh
    specs = [pl.BlockSpec((None, tr, cols), functools.partial(lambda i, pos_ref, j: (j, i, 0), j=j))
             for j in range(nb)]
    return pl.pallas_call(
        body, name=name,
        grid_spec=pltpu.PrefetchScalarGridSpec(
            num_scalar_prefetch=1, grid=(rh // tr,), in_specs=specs,
            out_specs=pl.BlockSpec((None, tr, cols), lambda i, pos_ref: (pos_ref[0], i, 0))),
        out_shape=jax.ShapeDtypeStruct((2, rh, cols), F32), compiler_params=_params(),
    )(pos_arr, gath, gath, gath, gath)


def _small_allreduce(v, name, rider=None, after=()):
    n = v.shape[0]
    n_dev = 8

    def body(v_ref, o_ref, buf, send_sems, recv_sems):
        x, y, c = _mesh_pos()
        me = 4 * x + 2 * y + c
        buf[me] = v_ref[...]
        sends = []
        peers = []
        for r in range(1, n_dev):
            px = 1 - x if r & 4 else x
            py = 1 - y if r & 2 else y
            pc = 1 - c if r & 1 else c
            peers.append((px, py, pc))
            cp = pltpu.make_async_remote_copy(
                src_ref=v_ref, dst_ref=buf.at[me], send_sem=send_sems.at[r - 1],
                recv_sem=recv_sems.at[r - 1], device_id=(px, py, pc), device_id_type=MESH)
            cp.start()
            sends.append(cp)
        for r, (px, py, pc) in enumerate(peers):
            pltpu.make_async_remote_copy(
                src_ref=v_ref, dst_ref=buf.at[4 * px + 2 * py + pc], send_sem=send_sems.at[r],
                recv_sem=recv_sems.at[r], device_id=(px, py, pc), device_id_type=MESH).wait_recv()
        for cp in sends:
            cp.wait_send()
        acc = buf[0]
        for i in range(1, n_dev):
            acc = acc + buf[i]
        o_ref[...] = acc

    whole = pl.BlockSpec(v.shape, lambda i: (0, 0))
    return _pallas(
        body, name=name, grid=(1,), in_specs=[whole], out_specs=whole,
        out_shape=jax.ShapeDtypeStruct(v.shape, v.dtype), operands=[v],
        scratch_shapes=[pltpu.VMEM((n_dev, n, LANES), F32), pltpu.SemaphoreType.DMA((n_dev - 1,)),
                        pltpu.SemaphoreType.DMA((n_dev - 1,))],
        rider=rider, after=after)


def _adamw_math(w, g, m, v):
    m = ADAM_B1 * m + (1.0 - ADAM_B1) * g
    v = ADAM_B2 * v + (1.0 - ADAM_B2) * (g * g)
    m_hat = m / (1.0 - ADAM_B1 ** ADAM_STEP)
    v_hat = v / (1.0 - ADAM_B2 ** ADAM_STEP)
    delta = -ADAM_LR * (m_hat / (jnp.sqrt(v_hat) + ADAM_EPS) + ADAM_WD * w)
    return delta, m, v


def _adamw(w, g, m, v, name, after=()):
    r, c = w.shape
    tr = 128 if r % 128 == 0 else 64
    assert r % tr == 0

    def body(w_ref, g_ref, m_ref, v_ref, go_ref, d_ref, mo_ref, vo_ref):
        gv = g_ref[...]
        d, mn, vn = _adamw_math(w_ref[...], gv, m_ref[...], v_ref[...])
        go_ref[...] = gv
        d_ref[...] = d
        mo_ref[...] = mn
        vo_ref[...] = vn

    blk = pl.BlockSpec((tr, c), lambda i: (i, 0))
    return _pallas(body, name=name, grid=(r // tr,), in_specs=[blk] * 4, out_specs=[blk] * 4,
                   out_shape=[jax.ShapeDtypeStruct((r, c), F32)] * 4, operands=[w, g, m, v], after=after)


def _adamw_small(ws, gs, ms, vs, name):
    n = len(ws)

    def body(*refs):
        w_r, g_r, m_r, v_r = refs[:n], refs[n:2 * n], refs[2 * n:3 * n], refs[3 * n:4 * n]
        d_o, m_o, v_o = refs[4 * n:5 * n], refs[5 * n:6 * n], refs[6 * n:7 * n]
        for i in range(n):
            d, mn, vn = _adamw_math(w_r[i][...], g_r[i][...], m_r[i][...], v_r[i][...])
            d_o[i][...] = d
            m_o[i][...] = mn
            v_o[i][...] = vn

    specs = [pl.BlockSpec(w.shape, lambda i: (0, 0)) for w in ws]
    shapes = [jax.ShapeDtypeStruct(w.shape, F32) for w in ws]
    outs = pl.pallas_call(
        body, name=name, grid=(1,), in_specs=specs * 4, out_specs=specs * 3, out_shape=shapes * 3,
        compiler_params=_params(),
    )(*ws, *gs, *ms, *vs)
    return outs[:n], outs[n:2 * n], outs[2 * n:]


BIG = ("w_in", "w_o_attn", "w_pw_conv", "w_out", "w_ffn_in", "w_ffn_out")
ROW_SHARDED = ("w_out", "w_ffn_out")
SMALL = ("norm1_w", "b_gate", "q_norm_w", "k_norm_w", "conv_w", "conv_b", "conv_ln_w", "conv_ln_b", "norm2_w")
ORDER = ("norm1_w", "w_in", "b_gate", "q_norm_w", "k_norm_w", "w_o_attn", "conv_w", "conv_b", "conv_ln_w",
         "conv_ln_b", "w_pw_conv", "w_out", "norm2_w", "w_ffn_in", "w_ffn_out")
PACK_TILE = 8 * LANES


def _pack_small(parts):
    rows = []
    for p in parts:
        flat = p.reshape(-1)
        pad = (-flat.shape[0]) % PACK_TILE
        rows.append(jnp.pad(flat, (0, pad)).reshape(-1, LANES))
    return jnp.concatenate(rows, axis=0)


def _unpack_small(packed, shapes):
    out, row = [], 0
    for shp in shapes:
        size = int(np.prod(shp))
        nrow = -(-size // PACK_TILE) * (PACK_TILE // LANES)
        out.append(packed[row:row + nrow].reshape(-1)[:size].reshape(shp))
        row += nrow
    return out


def kernel(x, positions, norm1_w, w_in, b_gate, q_norm_w, k_norm_w, w_o_attn, conv_w, conv_b, conv_ln_w, conv_ln_b, w_pw_conv, w_out, norm2_w, w_ffn_in, w_ffn_out, loss_target, m_norm1_w, m_w_in, m_b_gate, m_q_norm_w, m_k_norm_w, m_w_o_attn, m_conv_w, m_conv_b, m_conv_ln_w, m_conv_ln_b, m_w_pw_conv, m_w_out, m_norm2_w, m_w_ffn_in, m_w_ffn_out, v_norm1_w, v_w_in, v_b_gate, v_q_norm_w, v_k_norm_w, v_w_o_attn, v_conv_w, v_conv_b, v_conv_ln_w, v_conv_ln_b, v_w_pw_conv, v_w_out, v_norm2_w, v_w_ffn_in, v_w_ffn_out):
    w = dict(norm1_w=norm1_w, w_in=w_in, b_gate=b_gate, q_norm_w=q_norm_w, k_norm_w=k_norm_w, w_o_attn=w_o_attn,
             conv_w=conv_w, conv_b=conv_b, conv_ln_w=conv_ln_w, conv_ln_b=conv_ln_b, w_pw_conv=w_pw_conv,
             w_out=w_out, norm2_w=norm2_w, w_ffn_in=w_ffn_in, w_ffn_out=w_ffn_out)
    m = dict(norm1_w=m_norm1_w, w_in=m_w_in, b_gate=m_b_gate, q_norm_w=m_q_norm_w, k_norm_w=m_k_norm_w,
             w_o_attn=m_w_o_attn, conv_w=m_conv_w, conv_b=m_conv_b, conv_ln_w=m_conv_ln_w,
             conv_ln_b=m_conv_ln_b, w_pw_conv=m_w_pw_conv, w_out=m_w_out, norm2_w=m_norm2_w,
             w_ffn_in=m_w_ffn_in, w_ffn_out=m_w_ffn_out)
    v = dict(norm1_w=v_norm1_w, w_in=v_w_in, b_gate=v_b_gate, q_norm_w=v_q_norm_w, k_norm_w=v_k_norm_w,
             w_o_attn=v_w_o_attn, conv_w=v_conv_w, conv_b=v_conv_b, conv_ln_w=v_conv_ln_w,
             conv_ln_b=v_conv_ln_b, w_pw_conv=v_w_pw_conv, w_out=v_w_out, norm2_w=v_norm2_w,
             w_ffn_in=v_w_ffn_in, w_ffn_out=v_w_ffn_out)
    cx, cy, cc = _mesh_pos()
    chip = 2 * cx + cy

    chip_arr = chip.reshape(1).astype(jnp.int32)
    pos_arr = jnp.stack([cc, chip]).astype(jnp.int32)
    bufs = {}
    for n in BIG:
        buf = _cast_into_slot(w[n][0], chip_arr, BF16, f"cast_{n}")
        bufs[n] = buf.reshape(N_CHIPS, 2, buf.shape[1] // 2, buf.shape[2])
    small_bufs = [_cast_into_slot(w[n][0], chip_arr, F32, f"slot_{n}") for n in ("conv_w", "b_gate")]
    w_in_buf, conv_w_buf, b_gate_buf = _comm_call(_gather_both_legs_rider([bufs["w_in"]], small_bufs),
                                                  "allgather_w_in")
    wts = dict(w_in=w_in_buf.reshape(N_CHIPS, -1, w_in_buf.shape[3]),
               conv_w=conv_w_buf.transpose(1, 0, 2).reshape(CONV_WIDTH, -1),
               b_gate=b_gate_buf.transpose(1, 0, 2).reshape(2, 1, -1),
               norm1_w=norm1_w, q_norm_w=q_norm_w, k_norm_w=k_norm_w, conv_b=conv_b, conv_ln_w=conv_ln_w,
               conv_ln_b=conv_ln_b, norm2_w=norm2_w)

    loss, grad_x, g, reduced, w_in_in_flight = _forward_backward(
        x[0], positions.reshape(-1, 1), loss_target[0], wts, [bufs[n] for n in LATE_GATHER], pos_arr)
    grads = {n: b.reshape(-1, b.shape[2]) for n, b in reduced.items()}

    w_in_in_flight, started = w_in_in_flight
    delta, new_m, new_v = {}, {}, {}
    for n in EARLY_REDUCE:
        grads[n], delta[n], new_m[n], new_v[n] = _adamw(w[n][0], grads[n], m[n][0], v[n][0], f"adamw_{n}",
                                                        after=[started])
    small_parts = [loss] + [g[n] for n in SMALL]
    small_shapes = [p.shape for p in small_parts]
    summed = _small_allreduce(_pack_small(small_parts), "small_allreduce",
                              after=[delta[n] for n in EARLY_REDUCE])
    reduced = _unpack_small(summed, small_shapes)
    loss_total = reduced[0].reshape(())
    for n, r in zip(SMALL, reduced[1:]):
        grads[n] = r
    ch_shard = conv_w.shape[2]
    grads["conv_w"] = lax.dynamic_slice_in_dim(grads["conv_w"], chip * ch_shard, ch_shard, axis=1)
    d_shard = b_gate.shape[2]
    grads["b_gate"] = lax.dynamic_slice_in_dim(grads["b_gate"], chip * d_shard, d_shard, axis=1)

    (by_chip_w_in,), _ = _split_wait(w_in_in_flight, after=[delta[n] for n in EARLY_REDUCE] + [summed],
                                     name="grads_w_in_exchange_wait")
    half_w_in = _sum_chips(by_chip_w_in, pos_arr, "grads_chip_sum_w_in")
    (shard_w_in,) = _comm_call(_pair_gather_rider([half_w_in]), "grads_pair_gather_w_in")
    grads["w_in"], delta["w_in"], new_m["w_in"], new_v["w_in"] = _adamw(
        w["w_in"][0], shard_w_in.reshape(-1, shard_w_in.shape[2]), m["w_in"][0], v["w_in"][0], "adamw_w_in")
    flat2 = lambda a: a.reshape(-1, a.shape[-1])
    d_s, m_s, v_s = _adamw_small([flat2(w[n]) for n in SMALL], [flat2(grads[n]) for n in SMALL],
                                 [flat2(m[n]) for n in SMALL], [flat2(v[n]) for n in SMALL], "adamw_small")
    for i, n in enumerate(SMALL):
        delta[n], new_m[n], new_v[n] = d_s[i], m_s[i], v_s[i]

    shaped = lambda d, n: d[n].reshape(w[n].shape)
    return (loss_total, grad_x[None], *[shaped(grads, n) for n in ORDER], *[shaped(delta, n) for n in ORDER],
            *[shaped(new_m, n) for n in ORDER], *[shaped(new_v, n) for n in ORDER])
```

```python
import functools

import numpy as np
import jax
import jax.numpy as jnp
from jax import lax
from jax.experimental import pallas as pl
from jax.experimental.pallas import tpu as pltpu

F32 = jnp.float32
BF16 = jnp.bfloat16
MESH = pl.DeviceIdType.MESH
ANY = pl.BlockSpec(memory_space=pl.ANY)

HEAD_DIM = 64
N_SLOT_HEADS = 8
DILATIONS = (1, 4, 16)
HALF_SPAN = 64
ROPE_THETA = 500000.0
ROT_DIM = 16
CONV_WIDTH = 31
EPS = 1e-6
NEG_INF = -1e30
ADAM_LR, ADAM_B1, ADAM_B2, ADAM_EPS, ADAM_WD, ADAM_STEP = 0.001, 0.9, 0.999, 1e-08, 0.01, 10

LANES = 128
QBLK = 128
KWIN = QBLK + 2 * HALF_SPAN
VMEM_LIMIT = 48 * 1024 * 1024
N_CHIPS = 4


def _params(**kw):
    return pltpu.CompilerParams(vmem_limit_bytes=VMEM_LIMIT, **kw)


class _Rider:
    def __init__(self, operands, out_shapes, aliases, scratch, start, wait):
        self.operands, self.out_shapes, self.aliases = list(operands), list(out_shapes), dict(aliases)
        self.scratch, self.start, self.wait = list(scratch), start, wait


def _pallas(body, *, name, grid, in_specs, out_specs, out_shape, operands, scratch_shapes=(), aliases=None,
            rider=None, after=()):
    single = not isinstance(out_specs, (list, tuple))
    out_specs_l = [out_specs] if single else list(out_specs)
    out_shape_l = [out_shape] if single else list(out_shape)
    aliases = dict(aliases or {})

    def call(fn, all_in_specs, all_out_specs, all_out_shape, all_scratch, all_aliases, all_operands):
        return pl.pallas_call(
            fn, name=name, grid=grid, in_specs=all_in_specs, out_specs=all_out_specs, out_shape=all_out_shape,
            scratch_shapes=all_scratch, input_output_aliases=all_aliases, compiler_params=_params(),
        )(*all_operands)

    if rider is None:
        n_main = len(in_specs)

        def ordered(*refs):
            body(*refs[:n_main], *refs[n_main + len(after):])

        res = call(ordered if after else body, list(in_specs) + [ANY] * len(after), out_specs_l, out_shape_l,
                   list(scratch_shapes), aliases, list(operands) + list(after))
        return res[0] if single else res
    n_in, n_rin = len(in_specs), len(rider.operands)
    n_out, n_rout = len(out_specs_l), len(rider.out_shapes)
    n_sc = len(scratch_shapes)

    def wrapped(*refs):
        main_in, r_in = refs[:n_in], refs[n_in:n_in + n_rin]
        o0 = n_in + n_rin + len(after)
        main_out, r_out = refs[o0:o0 + n_out], refs[o0 + n_out:o0 + n_out + n_rout]
        s0 = o0 + n_out + n_rout
        main_sc, r_sc = refs[s0:s0 + n_sc], refs[s0 + n_sc:]
        ids = [pl.program_id(d) for d in range(len(grid))]
        first = functools.reduce(jnp.logical_and, [i == 0 for i in ids])
        last = functools.reduce(jnp.logical_and, [i == n - 1 for i, n in zip(ids, grid)])

        @pl.when(first)
        def _():
            rider.start(r_in, r_out, r_sc)

        body(*main_in, *main_out, *main_sc)

        @pl.when(last)
        def _():
            rider.wait(r_in, r_out, r_sc)

    for src, dst in rider.aliases.items():
        aliases[n_in + src] = n_out + dst
    res = call(wrapped, list(in_specs) + [ANY] * (n_rin + len(after)), out_specs_l + [ANY] * n_rout,
               out_shape_l + rider.out_shapes, list(scratch_shapes) + rider.scratch, aliases,
               list(operands) + rider.operands + list(after))
    main = res[:n_out]
    return (main[0] if single else main), res[n_out:]


def _matmul(a, b, *, mode, tm, tn, tk, out_dtype, name, b_blocked=False,
            out_blocked=None, cols_outer=False, rider=None, after=()):
    a_shape = a.shape
    if mode == "nn":
        m_dim, k_dim = a_shape
        n_dim = b.shape[0] * b.shape[2] if b_blocked else b.shape[1]
        rows, cols, red = m_dim, n_dim, k_dim
    elif mode == "nt":
        m_dim, n_dim = a_shape
        k_dim = b.shape[1] if b_blocked else b.shape[0]
        rows, cols, red = m_dim, k_dim, n_dim
    else:
        m_dim, k_dim = a_shape
        n_dim = b.shape[1]
        rows, cols, red = k_dim, n_dim, m_dim
    assert rows % tm == 0 and cols % tn == 0 and red % tk == 0, (name, rows, cols, red)
    ni, nj, nk = rows // tm, cols // tn, red // tk

    if mode == "nn":
        a_spec = pl.BlockSpec((tm, tk), lambda i, j, k: (i, k))
        if b_blocked:
            per = b.shape[2] // tn
            b_spec = pl.BlockSpec((None, tk, tn), lambda i, j, k: (j // per, k, j % per))
        else:
            b_spec = pl.BlockSpec((tk, tn), lambda i, j, k: (k, j))
        dims = (((1,), (0,)), ((), ()))
    elif mode == "nt":
        a_spec = pl.BlockSpec((tm, tk), lambda i, j, k: (i, k))
        if b_blocked:
            per = b.shape[2] // tk
            b_spec = pl.BlockSpec((None, tn, tk), lambda i, j, k: (k // per, j, k % per))
        else:
            b_spec = pl.BlockSpec((tn, tk), lambda i, j, k: (j, k))
        dims = (((1,), (1,)), ((), ()))
    else:
        a_spec = pl.BlockSpec((tk, tm), lambda i, j, k: (k, i))
        b_spec = pl.BlockSpec((tk, tn), lambda i, j, k: (k, j))
        dims = (((0,), (0,)), ((), ()))

    if out_blocked:
        per_o = (cols // out_blocked) // tn
        out_spec = pl.BlockSpec((None, tm, tn), lambda i, j, k: (j // per_o, i, j % per_o))
        out_shape = jax.ShapeDtypeStruct((out_blocked, rows, cols // out_blocked), out_dtype)
    else:
        out_spec = pl.BlockSpec((tm, tn), lambda i, j, k: (i, j))
        out_shape = jax.ShapeDtypeStruct((rows, cols), out_dtype)

    def body(a_ref, b_ref, o_ref, *acc):
        prod = lax.dot_general(a_ref[...], b_ref[...], dims, preferred_element_type=F32)
        if nk == 1:
            o_ref[...] = prod.astype(out_dtype)
        else:
            acc_ref, = acc
            k = pl.program_id(2)

            @pl.when(k == 0)
            def _():
                acc_ref[...] = prod

            @pl.when(k > 0)
            def _():
                acc_ref[...] += prod

            @pl.when(k == nk - 1)
            def _():
                o_ref[...] = acc_ref[...].astype(out_dtype)

    scratch = [pltpu.VMEM((tm, tn), F32)] if nk > 1 else []
    grid = (ni, nj, nk)
    if cols_outer:
        swap = lambda spec: pl.BlockSpec(spec.block_shape, lambda j, i, k, f=spec.index_map: f(i, j, k))
        a_spec, b_spec, out_spec, grid = swap(a_spec), swap(b_spec), swap(out_spec), (nj, ni, nk)
    return _pallas(body, name=name, grid=grid, in_specs=[a_spec, b_spec], out_specs=out_spec,
                   out_shape=out_shape, operands=[a, b], scratch_shapes=scratch, rider=rider, after=after)


def _rmsnorm_fwd(x, w, name):
    s, d = x.shape
    tm = 256

    def body(x_ref, w_ref, o_ref):
        xv = x_ref[...]
        rstd = lax.rsqrt(jnp.mean(xv * xv, axis=-1, keepdims=True) + EPS)
        o_ref[...] = (xv * rstd * w_ref[...]).astype(BF16)

    return pl.pallas_call(
        body, name=name, grid=(s // tm,),
        in_specs=[pl.BlockSpec((tm, d), lambda i: (i, 0)), pl.BlockSpec((1, d), lambda i: (0, 0))],
        out_specs=pl.BlockSpec((tm, d), lambda i: (i, 0)),
        out_shape=jax.ShapeDtypeStruct((s, d), BF16), compiler_params=_params(),
    )(x, w)


def _rmsnorm_bwd(dh, x, w, dres, name, rider=None):
    s, d = x.shape
    tm = 256

    def body(dh_ref, x_ref, w_ref, dres_ref, dx_ref, dxb_ref, dw_ref):
        xv = x_ref[...]
        rstd = lax.rsqrt(jnp.mean(xv * xv, axis=-1, keepdims=True) + EPS)
        xhat = xv * rstd
        dhv = dh_ref[...]
        g = dhv * w_ref[...]
        dx = rstd * (g - xhat * jnp.mean(g * xhat, axis=-1, keepdims=True)) + dres_ref[...]
        dx_ref[...] = dx
        dxb_ref[...] = dx.astype(BF16)
        part = jnp.sum(dhv * xhat, axis=0, keepdims=True)

        @pl.when(pl.program_id(0) == 0)
        def _():
            dw_ref[...] = part

        @pl.when(pl.program_id(0) > 0)
        def _():
            dw_ref[...] += part

    row = pl.BlockSpec((tm, d), lambda i: (i, 0))
    vec = pl.BlockSpec((1, d), lambda i: (0, 0))
    return _pallas(
        body, name=name, grid=(s // tm,), in_specs=[row, row, vec, row], out_specs=[row, row, vec],
        out_shape=[jax.ShapeDtypeStruct((s, d), F32), jax.ShapeDtypeStruct((s, d), BF16),
                   jax.ShapeDtypeStruct((1, d), F32)],
        operands=[dh, x, w, dres], rider=rider)


def _rope_consts():
    lane = np.arange(LANES)
    in_head = lane % HEAD_DIM
    inv_freq = ROPE_THETA ** (-jnp.arange(0, ROT_DIM, 2, dtype=F32) / ROT_DIM)
    invf = jnp.where(jnp.asarray(in_head < ROT_DIM), jnp.tile(inv_freq, LANES // (ROT_DIM // 2)), 0.0)
    m_a = np.where(in_head < ROT_DIM // 2, -1.0, 0.0).astype(np.float32)
    m_b = np.where((in_head >= ROT_DIM // 2) & (in_head < ROT_DIM), 1.0, 0.0).astype(np.float32)
    block_diag = (lane[:, None] // HEAD_DIM == lane[None, :] // HEAD_DIM).astype(np.float32)
    return (invf.reshape(1, LANES).astype(F32), jnp.asarray(m_a).reshape(1, LANES),
            jnp.asarray(m_b).reshape(1, LANES), jnp.asarray(block_diag, dtype=BF16))


def _head_sums(v, bd):
    hi = v.astype(BF16)
    lo = (v - hi.astype(F32)).astype(BF16)
    return jnp.dot(hi, bd, preferred_element_type=F32) + jnp.dot(lo, bd, preferred_element_type=F32)


def _qk_fwd(proj, pos_col, qw2, kw2, consts, name, rider=None):
    s = proj.shape[0]
    width = 3 * N_SLOT_HEADS * HEAD_DIM
    tm = 128
    invf, m_a, m_b, bd = consts
    scale = HEAD_DIM ** -0.5

    def body(q_ref, k_ref, pos_ref, qw_ref, kw_ref, invf_ref, ma_ref, mb_ref, bd_ref, qo_ref, ko_ref):
        ang = pos_ref[...].astype(F32) * invf_ref[...]
        cos = jnp.cos(ang)
        sin = jnp.sin(ang)
        s_a = sin * ma_ref[...]
        s_b = sin * mb_ref[...]
        bdv = bd_ref[...]
        for src, w_ref, dst, sc in ((q_ref, qw_ref, qo_ref, scale), (k_ref, kw_ref, ko_ref, 1.0)):
            for cb in range(width // LANES):
                cols = slice(cb * LANES, (cb + 1) * LANES)
                t = src[:, cols]
                rstd = lax.rsqrt(_head_sums(t * t, bdv) * (1.0 / HEAD_DIM) + EPS)
                y = t * rstd * w_ref[...]
                r = y * cos + pltpu.roll(y, LANES - 8, axis=1) * s_a + pltpu.roll(y, 8, axis=1) * s_b
                dst[:, cols] = r * sc if sc != 1.0 else r

    vec = pl.BlockSpec((1, LANES), lambda i: (0, 0))
    return _pallas(
        body, name=name, grid=(s // tm,),
        in_specs=[pl.BlockSpec((tm, width), lambda i: (i, 0)), pl.BlockSpec((tm, width), lambda i: (i, 1)),
                  pl.BlockSpec((tm, 1), lambda i: (i, 0)), vec, vec, vec, vec, vec,
                  pl.BlockSpec((LANES, LANES), lambda i: (0, 0))],
        out_specs=[pl.BlockSpec((tm, width), lambda i: (i, 0))] * 2,
        out_shape=[jax.ShapeDtypeStruct((s, width), F32)] * 2,
        operands=[proj, proj, pos_col, qw2, kw2, invf, m_a, m_b, bd], rider=rider)


def _qk_bwd(dqn, dkn, dv, da, db, dgl, proj, pos_col, qw2, kw2, consts, name, rider=None):
    s = proj.shape[0]
    width = 3 * N_SLOT_HEADS * HEAD_DIM
    ch = da.shape[1]
    gate_w = dgl.shape[1]
    out_w = 3 * width + 2 * ch + gate_w
    assert out_w == proj.shape[1]
    tm = 128
    invf, m_a, m_b, bd = consts
    scale = HEAD_DIM ** -0.5

    def body(dq_ref, dk_ref, dv_ref, da_ref, db_ref, dgl_ref, q_ref, k_ref, pos_ref, qw_ref, kw_ref,
             invf_ref, ma_ref, mb_ref, bd_ref, out_ref, dqw_ref, dkw_ref):
        ang = pos_ref[...].astype(F32) * invf_ref[...]
        cos = jnp.cos(ang)
        sin = jnp.sin(ang)
        s_a = sin * ma_ref[...]
        s_b = sin * mb_ref[...]
        bdv = bd_ref[...]
        first = pl.program_id(0) == 0
        for src, dsrc, w_ref, col0, dw_ref, sc in ((q_ref, dq_ref, qw_ref, 0, dqw_ref, scale),
                                                   (k_ref, dk_ref, kw_ref, width, dkw_ref, 1.0)):
            dw_acc = jnp.zeros((1, LANES), F32)
            for cb in range(width // LANES):
                cols = slice(cb * LANES, (cb + 1) * LANES)
                t = src[:, cols]
                dr = dsrc[:, cols]
                if sc != 1.0:
                    dr = dr * sc
                dy = dr * cos + pltpu.roll(dr * s_a, 8, axis=1) + pltpu.roll(dr * s_b, LANES - 8, axis=1)
                rstd = lax.rsqrt(_head_sums(t * t, bdv) * (1.0 / HEAD_DIM) + EPS)
                xhat = t * rstd
                g = dy * w_ref[...]
                dt = rstd * (g - xhat * (_head_sums(g * xhat, bdv) * (1.0 / HEAD_DIM)))
                out_ref[:, col0 + cb * LANES: col0 + (cb + 1) * LANES] = dt.astype(BF16)
                dw_acc = dw_acc + jnp.sum(dy * xhat, axis=0, keepdims=True)
            dw_acc = dw_acc + pltpu.roll(dw_acc, HEAD_DIM, axis=1)

            @pl.when(first)
            def _(dw_ref=dw_ref, dw_acc=dw_acc):
                dw_ref[...] = dw_acc

            @pl.when(jnp.logical_not(first))
            def _(dw_ref=dw_ref, dw_acc=dw_acc):
                dw_ref[...] += dw_acc
        out_ref[:, 2 * width: 3 * width] = dv_ref[...].astype(BF16)
        out_ref[:, 3 * width: 3 * width + ch] = da_ref[...]
        out_ref[:, 3 * width + ch: 3 * width + 2 * ch] = db_ref[...]
        out_ref[:, 3 * width + 2 * ch: out_w] = dgl_ref[...]

    vec = pl.BlockSpec((1, LANES), lambda i: (0, 0))
    blk = lambda c: pl.BlockSpec((tm, width), lambda i: (i, c))
    cblk = pl.BlockSpec((tm, ch), lambda i: (i, 0))
    return _pallas(
        body, name=name, grid=(s // tm,),
        in_specs=[blk(0), blk(0), blk(0), cblk, cblk, pl.BlockSpec((tm, gate_w), lambda i: (i, 0)),
                  blk(0), blk(1), pl.BlockSpec((tm, 1), lambda i: (i, 0)), vec, vec, vec, vec, vec,
                  pl.BlockSpec((LANES, LANES), lambda i: (0, 0))],
        out_specs=[pl.BlockSpec((tm, out_w), lambda i: (i, 0)), vec, vec],
        out_shape=[jax.ShapeDtypeStruct((s, out_w), BF16)] + [jax.ShapeDtypeStruct((1, LANES), F32)] * 2,
        operands=[dqn, dkn, dv, da, db, dgl, proj, proj, pos_col, qw2, kw2, invf, m_a, m_b, bd],
        rider=rider)


def _row_chunks(n_rows, fn, chunk=256):
    def step(i, c):
        fn(pl.ds(pl.multiple_of(i * chunk, chunk), chunk))
        return c
    lax.fori_loop(0, n_rows // chunk, step, 0)


def _to_residue_major(dst, src, s, d, dst_off=0, cast=None):
    seq = s // d
    for r in range(d):
        v = src[...] if d == 1 else src[pl.ds(r, seq, stride=d), :]
        dst[dst_off + r * seq: dst_off + (r + 1) * seq, :] = v if cast is None else v.astype(cast)


def _from_residue_major(dst, src, s, d, src_off=0):
    seq = s // d
    for r in range(d):
        v = src[src_off + r * seq: src_off + (r + 1) * seq, :]
        if d == 1:
            dst[...] = v
        else:
            dst[pl.ds(r, seq, stride=d), :] = v


def _band_bias():
    qi = lax.broadcasted_iota(jnp.int32, (QBLK, KWIN), 0)
    kj = lax.broadcasted_iota(jnp.int32, (QBLK, KWIN), 1)
    return jnp.where(jnp.abs(kj - HALF_SPAN - qi) <= HALF_SPAN, 0.0, NEG_INF).astype(F32)


def _range_bias(base, seq):
    kj = lax.broadcasted_iota(jnp.int32, (1, KWIN), 1)
    lo = (base & -seq) - base + HALF_SPAN
    return jnp.where((kj >= lo) & (kj < lo + seq), 0.0, NEG_INF).astype(F32)


def _skewed_blocks(n_blk, produce, consume):
    produce(0, 0)
    for b in range(n_blk):
        consume(b, b % 2)
        if b + 1 < n_blk:
            produce(b + 1, (b + 1) % 2)


def _block_base(b):
    return b * QBLK if isinstance(b, int) else pl.multiple_of(b * QBLK, QBLK)


def _attn_fwd(qn, kn, proj, name, rider=None):
    s = qn.shape[0]
    n_pairs = N_SLOT_HEADS * HEAD_DIM // LANES
    v_col0 = 2 * qn.shape[1] // LANES
    nt_dims = (((1,), (1,)), ((), ()))

    def body(q_ref, k_ref, v_ref, attn_ref, lse_ref, attn_b_ref, q_rm, k_rm, v_rm, acc_rm, m_rm, l_rm,
             acc_p, m_p, l_p, m_run, l_run, acc_run, band, s_buf, m_buf):
        g = pl.program_id(1)
        zpad = jnp.zeros((HALF_SPAN, LANES), BF16)
        k_rm[0:HALF_SPAN, :] = zpad
        k_rm[s + HALF_SPAN: s + 2 * HALF_SPAN, :] = zpad
        v_rm[0:HALF_SPAN, 0:LANES] = zpad
        v_rm[s + HALF_SPAN: s + 2 * HALF_SPAN, 0:LANES] = zpad

        def ones_rows(rows):
            v_rm[pl.ds(rows.start, rows.size), LANES:2 * LANES] = jnp.ones((rows.size, LANES), BF16)

        _row_chunks(s + 2 * HALF_SPAN, ones_rows, chunk=2 * HALF_SPAN)
        band[...] = _band_bias()
        lane = lax.broadcasted_iota(jnp.int32, (QBLK, LANES), 1)
        low = lane < HEAD_DIM
        n_blk = s // QBLK

        for gi, d in enumerate(DILATIONS):
            @pl.when(g == gi)
            def _(gi=gi, d=d):
                seq = s // d
                _to_residue_major(q_rm, q_ref, s, d, cast=BF16)
                _to_residue_major(k_rm, k_ref, s, d, dst_off=HALF_SPAN, cast=BF16)
                _to_residue_major(v_rm.at[:, 0:LANES], v_ref, s, d, dst_off=HALF_SPAN, cast=BF16)

                def scores(b, slot):
                    base = _block_base(b)
                    q = q_rm[pl.ds(base, QBLK), :]
                    zero = jnp.zeros_like(q)
                    q2 = jnp.concatenate([jnp.where(low, q, zero), jnp.where(low, zero, q)], axis=0)
                    sc = lax.dot_general(q2, k_rm[pl.ds(base, KWIN), :], nt_dims, preferred_element_type=F32)
                    bias = band[...] + _range_bias(base, seq)
                    for hh in range(2):
                        rows = slice(hh * QBLK, (hh + 1) * QBLK)
                        sh = sc[rows, :] + bias
                        s_buf[slot, rows, :] = sh
                        m_buf[slot, rows, :] = jnp.broadcast_to(jnp.max(sh, axis=-1, keepdims=True), (QBLK, LANES))

                def outputs(b, slot):
                    base = _block_base(b)
                    sv = s_buf[slot]
                    mb = m_buf[slot]
                    p = jnp.exp(jnp.concatenate([sv[:, 0:LANES] - mb, sv[:, LANES:2 * LANES] - mb], axis=1))
                    pv = jnp.dot(p.astype(BF16), v_rm[pl.ds(base, KWIN), :], preferred_element_type=F32)
                    rows = pl.ds(base, QBLK)
                    acc_rm[rows, :] = jnp.where(low, pv[0:QBLK, 0:LANES], pv[QBLK:2 * QBLK, 0:LANES])
                    l_rm[rows, :] = jnp.where(low, pv[0:QBLK, LANES:2 * LANES], pv[QBLK:2 * QBLK, LANES:2 * LANES])
                    m_rm[rows, :] = jnp.where(low, mb[0:QBLK, :], mb[QBLK:2 * QBLK, :])

                _skewed_blocks(n_blk, scores, outputs)
                if d == 1:
                    src = (acc_rm, m_rm, l_rm)
                else:
                    for dst_, src_ in ((acc_p, acc_rm), (m_p, m_rm), (l_p, l_rm)):
                        _from_residue_major(dst_, src_, s, d)
                    src = (acc_p, m_p, l_p)

                def combine(rows):
                    a_g, m_g, l_g = src[0][rows, :], src[1][rows, :], src[2][rows, :]
                    if gi == 0:
                        m_new, l_new, a_new = m_g, l_g, a_g
                    else:
                        m_old = m_run[rows, :]
                        m_new = jnp.maximum(m_old, m_g)
                        w_old = jnp.exp(m_old - m_new)
                        w_g = jnp.exp(m_g - m_new)
                        l_new = l_run[rows, :] * w_old + l_g * w_g
                        a_new = acc_run[rows, :] * w_old + a_g * w_g
                    if gi == len(DILATIONS) - 1:
                        out = a_new / l_new
                        attn_ref[rows, :] = out
                        attn_b_ref[rows, :] = out.astype(BF16)
                        lse_ref[rows, :] = m_new + jnp.log(l_new)
                    else:
                        m_run[rows, :] = m_new
                        l_run[rows, :] = l_new
                        acc_run[rows, :] = a_new

                _row_chunks(s, combine)

    qk_spec = pl.BlockSpec((s, LANES), lambda hp, g: (0, g * n_pairs + hp))
    v_spec = pl.BlockSpec((s, LANES), lambda hp, g: (0, v_col0 + g * n_pairs + hp))
    o_spec = pl.BlockSpec((s, LANES), lambda hp, g: (0, hp))
    f32buf = pltpu.VMEM((s, LANES), F32)
    return _pallas(
        body, name=name, grid=(n_pairs, len(DILATIONS)), in_specs=[qk_spec, qk_spec, v_spec],
        out_specs=[o_spec, o_spec, o_spec],
        out_shape=[jax.ShapeDtypeStruct((s, n_pairs * LANES), F32)] * 2
        + [jax.ShapeDtypeStruct((s, n_pairs * LANES), BF16)],
        operands=[qn, kn, proj],
        scratch_shapes=[pltpu.VMEM((s, LANES), BF16), pltpu.VMEM((s + 2 * HALF_SPAN, LANES), BF16),
                        pltpu.VMEM((s + 2 * HALF_SPAN, 2 * LANES), BF16)] + [f32buf] * 9
        + [pltpu.VMEM((QBLK, KWIN), F32), pltpu.VMEM((2, 2 * QBLK, KWIN), F32),
           pltpu.VMEM((2, 2 * QBLK, LANES), F32)],
        rider=rider)


def _attn_bwd(qn, kn, proj, dattn, attn, lse, bd, name, rider=None):
    s = qn.shape[0]
    n_pairs = N_SLOT_HEADS * HEAD_DIM // LANES
    v_col0 = 2 * qn.shape[1] // LANES
    nt_dims = (((1,), (1,)), ((), ()))
    tn_dims = (((0,), (0,)), ((), ()))
    spad = s + 2 * HALF_SPAN

    def body(q_ref, k_ref, v_ref, do_ref, o_ref, lse_ref, bd_ref, dq_ref, dk_ref, dv_ref,
             q_rm, k_rm, v_rm, do_rm, lse0_rm, lse1_rm, dd0_rm, dd1_rm, dq_rm, dk_rm, dv_rm,
             lse0_p, lse1_p, dd0_p, dd1_p, band, p_buf, ds_buf):
        g = pl.program_id(1)
        zpad = jnp.zeros((HALF_SPAN, LANES), BF16)
        for buf in (k_rm, v_rm):
            buf[0:HALF_SPAN, :] = zpad
            buf[s + HALF_SPAN: spad, :] = zpad
        zf = jnp.zeros((HALF_SPAN, LANES), F32)
        for buf in (dk_rm, dv_rm):
            buf[0:HALF_SPAN, :] = zf
            buf[s + HALF_SPAN: spad, :] = zf
        band[...] = _band_bias()

        def clear(rows):
            z = jnp.zeros((rows.size, LANES), F32)
            dk_rm[pl.ds(rows.start + HALF_SPAN, rows.size), :] = z
            dv_rm[pl.ds(rows.start + HALF_SPAN, rows.size), :] = z

        _row_chunks(s, clear)

        def prepare(rows):
            lo = lax.broadcasted_iota(jnp.int32, (rows.size, LANES), 1) < HEAD_DIM
            dsum = _head_sums(do_ref[rows, :] * o_ref[rows, :], bd_ref[...])
            dswap = pltpu.roll(dsum, HEAD_DIM, axis=1)
            dd0_p[rows, :] = jnp.where(lo, dsum, dswap)
            dd1_p[rows, :] = jnp.where(lo, dswap, dsum)
            lv = lse_ref[rows, :]
            lswap = pltpu.roll(lv, HEAD_DIM, axis=1)
            lse0_p[rows, :] = jnp.where(lo, lv, lswap)
            lse1_p[rows, :] = jnp.where(lo, lswap, lv)

        @pl.when(g == 0)
        def _():
            _row_chunks(s, prepare)
        lane = lax.broadcasted_iota(jnp.int32, (QBLK, LANES), 1)
        low = lane < HEAD_DIM
        n_blk = s // QBLK

        def stacked(ref, rows):
            val = ref[rows, :]
            zero = jnp.zeros_like(val)
            return jnp.concatenate([jnp.where(low, val, zero), jnp.where(low, zero, val)], axis=0)

        for gi, d in enumerate(DILATIONS):
            @pl.when(g == gi)
            def _(d=d):
                seq = s // d
                _to_residue_major(q_rm, q_ref, s, d, cast=BF16)
                _to_residue_major(k_rm, k_ref, s, d, dst_off=HALF_SPAN, cast=BF16)
                _to_residue_major(v_rm, v_ref, s, d, dst_off=HALF_SPAN, cast=BF16)
                _to_residue_major(do_rm, do_ref, s, d, cast=BF16)
                for dst_, src_ in ((lse0_rm, lse0_p), (lse1_rm, lse1_p), (dd0_rm, dd0_p), (dd1_rm, dd1_p)):
                    _to_residue_major(dst_, src_, s, d)

                def scores(b, slot):
                    base = _block_base(b)
                    rows = pl.ds(base, QBLK)
                    win = pl.ds(base, KWIN)
                    sc = lax.dot_general(stacked(q_rm, rows), k_rm[win, :], nt_dims, preferred_element_type=F32)
                    dp = lax.dot_general(stacked(do_rm, rows), v_rm[win, :], nt_dims, preferred_element_type=F32)
                    bias = band[...] + _range_bias(base, seq)
                    for hh, (lse_r, dd_r) in enumerate(((lse0_rm, dd0_rm), (lse1_rm, dd1_rm))):
                        r = slice(hh * QBLK, (hh + 1) * QBLK)
                        lse_h = lse_r[rows, :]
                        dd_h = dd_r[rows, :]
                        sh = sc[r, :] + bias
                        p = jnp.exp(jnp.concatenate([sh[:, 0:LANES] - lse_h, sh[:, LANES:KWIN] - lse_h], axis=1))
                        dph = dp[r, :]
                        ds = p * jnp.concatenate([dph[:, 0:LANES] - dd_h, dph[:, LANES:KWIN] - dd_h], axis=1)
                        p_buf[slot, r, :] = p.astype(BF16)
                        ds_buf[slot, r, :] = ds.astype(BF16)

                def grads(b, slot):
                    base = _block_base(b)
                    rows = pl.ds(base, QBLK)
                    win = pl.ds(base, KWIN)
                    p = p_buf[slot]
                    ds = ds_buf[slot]
                    dq2 = jnp.dot(ds, k_rm[win, :], preferred_element_type=F32)
                    dq_rm[rows, :] = jnp.where(low, dq2[0:QBLK, :], dq2[QBLK:2 * QBLK, :])
                    dk_rm[win, :] += lax.dot_general(ds, stacked(q_rm, rows), tn_dims, preferred_element_type=F32)
                    dv_rm[win, :] += lax.dot_general(p, stacked(do_rm, rows), tn_dims, preferred_element_type=F32)

                _skewed_blocks(n_blk, scores, grads)
                _from_residue_major(dq_ref, dq_rm, s, d)
                _from_residue_major(dk_ref, dk_rm, s, d, src_off=HALF_SPAN)
                _from_residue_major(dv_ref, dv_rm, s, d, src_off=HALF_SPAN)

    qk_spec = pl.BlockSpec((s, LANES), lambda hp, g: (0, g * n_pairs + hp))
    v_spec = pl.BlockSpec((s, LANES), lambda hp, g: (0, v_col0 + g * n_pairs + hp))
    o_spec = pl.BlockSpec((s, LANES), lambda hp, g: (0, hp))
    width = qn.shape[1]
    f32buf = pltpu.VMEM((s, LANES), F32)
    f32pad = pltpu.VMEM((spad, LANES), F32)
    return _pallas(
        body, name=name, grid=(n_pairs, len(DILATIONS)),
        in_specs=[qk_spec, qk_spec, v_spec, o_spec, o_spec, o_spec,
                  pl.BlockSpec((LANES, LANES), lambda hp, g: (0, 0))],
        out_specs=[qk_spec, qk_spec, qk_spec],
        out_shape=[jax.ShapeDtypeStruct((s, width), F32)] * 3,
        operands=[qn, kn, proj, dattn, attn, lse, bd],
        scratch_shapes=[pltpu.VMEM((s, LANES), BF16), pltpu.VMEM((spad, LANES), BF16),
                        pltpu.VMEM((spad, LANES), BF16), pltpu.VMEM((s, LANES), BF16),
                        f32buf, f32buf, f32buf, f32buf, f32buf, f32pad, f32pad,
                        f32buf, f32buf, f32buf, f32buf, pltpu.VMEM((QBLK, KWIN), F32),
                        pltpu.VMEM((2, 2 * QBLK, KWIN), BF16), pltpu.VMEM((2, 2 * QBLK, KWIN), BF16)],
        rider=rider)


CONV_PAD = 16


def _conv_fwd(proj, conv_w, conv_b, col0, name, rider=None):
    s = proj.shape[0]
    ch = conv_w.shape[1]
    nblk = ch // LANES
    a0 = col0 // LANES
    tr = 256
    shift = CONV_PAD - (CONV_WIDTH - 1) // 2

    def body(a_ref, b_ref, w_ref, bias_ref, u0_ref, uc_ref, pad):
        z = jnp.zeros((CONV_PAD, LANES), F32)
        pad[0:CONV_PAD, :] = z
        pad[s + CONV_PAD: s + 2 * CONV_PAD, :] = z

        def glu(rows):
            u0 = a_ref[rows, :] * jax.nn.sigmoid(b_ref[rows, :])
            u0_ref[rows, :] = u0
            pad[pl.ds(rows.start + CONV_PAD, rows.size), :] = u0

        _row_chunks(s, glu)
        for t in range(0, s, tr):
            acc = jnp.broadcast_to(bias_ref[...], (tr, LANES))
            for k in range(CONV_WIDTH):
                acc = acc + w_ref[k:k + 1, :] * pad[t + k + shift: t + k + shift + tr, :]
            uc_ref[t:t + tr, :] = acc

    return _pallas(
        body, name=name, grid=(nblk,),
        in_specs=[pl.BlockSpec((s, LANES), lambda c: (0, a0 + c)),
                  pl.BlockSpec((s, LANES), lambda c: (0, a0 + nblk + c)),
                  pl.BlockSpec((CONV_WIDTH, LANES), lambda c: (0, c)),
                  pl.BlockSpec((1, LANES), lambda c: (0, c))],
        out_specs=[pl.BlockSpec((s, LANES), lambda c: (0, c))] * 2,
        out_shape=[jax.ShapeDtypeStruct((s, ch), F32)] * 2, operands=[proj, proj, conv_w, conv_b],
        scratch_shapes=[pltpu.VMEM((s + 2 * CONV_PAD, LANES), F32)], rider=rider)


def _ln_silu_fwd(uc, ln_w, ln_b, name):
    s, ch = uc.shape
    tm = 256

    def body(u_ref, w_ref, b_ref, o_ref):
        u = u_ref[...]
        mu = jnp.mean(u, axis=-1, keepdims=True)
        xc = u - mu
        rstd = lax.rsqrt(jnp.mean(xc * xc, axis=-1, keepdims=True) + EPS)
        z = xc * rstd * w_ref[...] + b_ref[...]
        o_ref[...] = (z * jax.nn.sigmoid(z)).astype(BF16)

    row = pl.BlockSpec((tm, ch), lambda i: (i, 0))
    vec = pl.BlockSpec((1, ch), lambda i: (0, 0))
    return pl.pallas_call(
        body, name=name, grid=(s // tm,), in_specs=[row, vec, vec], out_specs=row,
        out_shape=jax.ShapeDtypeStruct((s, ch), BF16), compiler_params=_params(),
    )(uc, ln_w, ln_b)


def _ln_silu_bwd(du3, uc, ln_w, ln_b, name):
    s, ch = uc.shape
    tm = 256

    def body(d_ref, u_ref, w_ref, b_ref, du_ref, dw_ref, db_ref):
        u = u_ref[...]
        mu = jnp.mean(u, axis=-1, keepdims=True)
        xc = u - mu
        rstd = lax.rsqrt(jnp.mean(xc * xc, axis=-1, keepdims=True) + EPS)
        xhat = xc * rstd
        z = xhat * w_ref[...] + b_ref[...]
        sg = jax.nn.sigmoid(z)
        dz = d_ref[...] * (sg * (1.0 + z * (1.0 - sg)))
        dxh = dz * w_ref[...]
        du_ref[...] = rstd * (dxh - jnp.mean(dxh, axis=-1, keepdims=True)
                              - xhat * jnp.mean(dxh * xhat, axis=-1, keepdims=True))
        pw = jnp.sum(dz * xhat, axis=0, keepdims=True)
        pb = jnp.sum(dz, axis=0, keepdims=True)
        first = pl.program_id(0) == 0

        @pl.when(first)
        def _():
            dw_ref[...] = pw
            db_ref[...] = pb

        @pl.when(jnp.logical_not(first))
        def _():
            dw_ref[...] += pw
            db_ref[...] += pb

    row = pl.BlockSpec((tm, ch), lambda i: (i, 0))
    vec = pl.BlockSpec((1, ch), lambda i: (0, 0))
    return pl.pallas_call(
        body, name=name, grid=(s // tm,), in_specs=[row, row, vec, vec], out_specs=[row, vec, vec],
        out_shape=[jax.ShapeDtypeStruct((s, ch), F32), jax.ShapeDtypeStruct((1, ch), F32),
                   jax.ShapeDtypeStruct((1, ch), F32)],
        compiler_params=_params(),
    )(du3, uc, ln_w, ln_b)


def _conv_bwd(duc, u0, proj, conv_w, col0, name, rider=None):
    s = proj.shape[0]
    ch = conv_w.shape[1]
    nblk = ch // LANES
    a0 = col0 // LANES
    tr = 256
    half = (CONV_WIDTH - 1) // 2
    shift = CONV_PAD - half

    def body(duc_ref, u0_ref, a_ref, b_ref, w_ref, da_ref, db_ref, dw_ref, dbias_ref, pad_d, pad_u):
        z = jnp.zeros((CONV_PAD, LANES), F32)
        for buf in (pad_d, pad_u):
            buf[0:CONV_PAD, :] = z
            buf[s + CONV_PAD: s + 2 * CONV_PAD, :] = z

        def fill(rows):
            dst = pl.ds(rows.start + CONV_PAD, rows.size)
            pad_d[dst, :] = duc_ref[rows, :]
            pad_u[dst, :] = u0_ref[rows, :]

        _row_chunks(s, fill)
        dw_acc = [jnp.zeros((8, LANES), F32) for _ in range(CONV_WIDTH)]
        dbias_acc = jnp.zeros((8, LANES), F32)
        for t in range(0, s, tr):
            d_t = duc_ref[t:t + tr, :]
            dbias_acc = dbias_acc + jnp.sum(d_t.reshape(tr // 8, 8, LANES), axis=0)
            du0 = jnp.zeros((tr, LANES), F32)
            for k in range(CONV_WIDTH):
                du0 = du0 + w_ref[k:k + 1, :] * pad_d[t - k + half + CONV_PAD: t - k + half + CONV_PAD + tr, :]
                prod = d_t * pad_u[t + k + shift: t + k + shift + tr, :]
                dw_acc[k] = dw_acc[k] + jnp.sum(prod.reshape(tr // 8, 8, LANES), axis=0)
            av = a_ref[t:t + tr, :]
            sg = jax.nn.sigmoid(b_ref[t:t + tr, :])
            da_ref[t:t + tr, :] = (du0 * sg).astype(BF16)
            db_ref[t:t + tr, :] = (du0 * av * sg * (1.0 - sg)).astype(BF16)
        for k in range(CONV_WIDTH):
            dw_ref[k:k + 1, :] = jnp.sum(dw_acc[k], axis=0, keepdims=True)
        dbias_ref[...] = jnp.sum(dbias_acc, axis=0, keepdims=True)

    col = lambda off: pl.BlockSpec((s, LANES), lambda c: (0, off + c))
    return _pallas(
        body, name=name, grid=(nblk,),
        in_specs=[col(0), col(0), col(a0), col(a0 + nblk),
                  pl.BlockSpec((CONV_WIDTH, LANES), lambda c: (0, c))],
        out_specs=[col(0), col(0), pl.BlockSpec((CONV_WIDTH, LANES), lambda c: (0, c)),
                   pl.BlockSpec((1, LANES), lambda c: (0, c))],
        out_shape=[jax.ShapeDtypeStruct((s, ch), BF16)] * 2
        + [jax.ShapeDtypeStruct((CONV_WIDTH, ch), F32), jax.ShapeDtypeStruct((1, ch), F32)],
        operands=[duc, u0, proj, proj, conv_w],
        scratch_shapes=[pltpu.VMEM((s + 2 * CONV_PAD, LANES), F32)] * 2, rider=rider)


GATE_BLK = 512


def _gate_fwd(proj, bg, y_a, y_b, col0, name):
    s, d = y_a.shape
    tm = 256
    g0 = col0 // GATE_BLK
    nb = d // GATE_BLK

    def body(ga_ref, gb_ref, ba_ref, bb_ref, ya_ref, yb_ref, o_ref):
        g_a = jax.nn.sigmoid(ga_ref[...] + ba_ref[...])
        g_b = jax.nn.sigmoid(gb_ref[...] + bb_ref[...])
        o_ref[...] = (g_a * ya_ref[...] + g_b * yb_ref[...]).astype(BF16)

    act = pl.BlockSpec((tm, GATE_BLK), lambda i, j: (i, j))
    return pl.pallas_call(
        body, name=name, grid=(s // tm, nb),
        in_specs=[pl.BlockSpec((tm, GATE_BLK), lambda i, j: (i, g0 + j)),
                  pl.BlockSpec((tm, GATE_BLK), lambda i, j: (i, g0 + nb + j)),
                  pl.BlockSpec((None, 1, GATE_BLK), lambda i, j: (0, 0, j)),
                  pl.BlockSpec((None, 1, GATE_BLK), lambda i, j: (1, 0, j)), act, act],
        out_specs=act, out_shape=jax.ShapeDtypeStruct((s, d), BF16), compiler_params=_params(),
    )(proj, proj, bg, bg, y_a, y_b)


def _out_proj_bwd_gates(dx1, w_out, proj, bg, y_a, y_b, col0, name, after=()):
    s, d = y_a.shape
    tm = 256
    half = d // 2
    assert col0 % half == 0
    c0 = col0 // half
    nt_dims = (((1,), (1,)), ((), ()))

    def body(dx_ref, w_ref, a0_ref, a1_ref, b0_ref, b1_ref, bias_ref, ya_ref, yb_ref,
             dgl_ref, dya_ref, dyb_ref, db_ref):
        dm = lax.dot_general(dx_ref[...], w_ref[...], nt_dims, preferred_element_type=F32)
        parts = []
        for br, (lo_ref, hi_ref, y_ref, dy_ref) in enumerate(((a0_ref, a1_ref, ya_ref, dya_ref),
                                                              (b0_ref, b1_ref, yb_ref, dyb_ref))):
            logits = jnp.concatenate([lo_ref[...], hi_ref[...]], axis=1)
            gate = jax.nn.sigmoid(logits + bias_ref[br])
            dy_ref[...] = (dm * gate).astype(BF16)
            dgl = dm * y_ref[...] * gate * (1.0 - gate)
            dgl_ref[:, br * d:(br + 1) * d] = dgl.astype(BF16)
            parts.append(jnp.sum(dgl, axis=0, keepdims=True))
        part = jnp.concatenate(parts, axis=0)
        first = pl.program_id(0) == 0

        @pl.when(first)
        def _():
            db_ref[...] = part

        @pl.when(jnp.logical_not(first))
        def _():
            db_ref[...] += part

    row = pl.BlockSpec((tm, d), lambda i: (i, 0))
    logit_blk = lambda k: pl.BlockSpec((tm, half), functools.partial(lambda i, k: (i, c0 + k), k=k))
    return _pallas(
        body, name=name, grid=(s // tm,),
        in_specs=[row, pl.BlockSpec((d, d), lambda i: (0, 0)), logit_blk(0), logit_blk(1), logit_blk(2),
                  logit_blk(3), pl.BlockSpec((2, 1, d), lambda i: (0, 0, 0)), row, row],
        out_specs=[pl.BlockSpec((tm, 2 * d), lambda i: (i, 0)), row, row, pl.BlockSpec((2, d), lambda i: (0, 0))],
        out_shape=[jax.ShapeDtypeStruct((s, 2 * d), BF16), jax.ShapeDtypeStruct((s, d), BF16),
                   jax.ShapeDtypeStruct((s, d), BF16), jax.ShapeDtypeStruct((2, d), F32)],
        operands=[dx1, w_out, proj, proj, proj, proj, bg, y_a, y_b], after=after)


def _ffn_in_swiglu(h2, w_blocked, name):
    s, k = h2.shape
    nblk, _, tn = w_blocked.shape
    ff = nblk // 2 * tn
    tm = 512

    def body(a_ref, wg_ref, wu_ref, g_ref, u_ref, act_ref):
        a = a_ref[...]
        gt = jnp.dot(a, wg_ref[...], preferred_element_type=F32)
        up = jnp.dot(a, wu_ref[...], preferred_element_type=F32)
        g_ref[...] = gt
        u_ref[...] = up
        act_ref[...] = (gt * jax.nn.sigmoid(gt) * up).astype(BF16)

    out = pl.BlockSpec((tm, tn), lambda j, i: (i, j))
    return pl.pallas_call(
        body, name=name, grid=(nblk // 2, s // tm),
        in_specs=[pl.BlockSpec((tm, k), lambda j, i: (i, 0)),
                  pl.BlockSpec((None, k, tn), lambda j, i: (j, 0, 0)),
                  pl.BlockSpec((None, k, tn), lambda j, i: (nblk // 2 + j, 0, 0))],
        out_specs=[out, out, out],
        out_shape=[jax.ShapeDtypeStruct((s, ff), F32), jax.ShapeDtypeStruct((s, ff), F32),
                   jax.ShapeDtypeStruct((s, ff), BF16)],
        compiler_params=_params(),
    )(h2, w_blocked, w_blocked)


def _ffn_out_bwd_swiglu(dy, w_ffn_out, gate, up, name, rider=None):
    s, d = dy.shape
    ff = gate.shape[1]
    tm = 256
    nt_dims = (((1,), (1,)), ((), ()))

    def body(dy_ref, w_ref, g_ref, u_ref, o_ref):
        dv = lax.dot_general(dy_ref[...], w_ref[...], nt_dims, preferred_element_type=F32)
        gt = g_ref[...]
        sg = jax.nn.sigmoid(gt)
        o_ref[:, 0:ff] = (dv * u_ref[...] * (sg * (1.0 + gt * (1.0 - sg)))).astype(BF16)
        o_ref[:, ff:2 * ff] = (dv * gt * sg).astype(BF16)

    row = pl.BlockSpec((tm, ff), lambda i: (i, 0))
    return _pallas(
        body, name=name, grid=(s // tm,),
        in_specs=[pl.BlockSpec((tm, d), lambda i: (i, 0)), pl.BlockSpec((ff, d), lambda i: (0, 0)), row, row],
        out_specs=pl.BlockSpec((tm, 2 * ff), lambda i: (i, 0)),
        out_shape=jax.ShapeDtypeStruct((s, 2 * ff), BF16), operands=[dy, w_ffn_out, gate, up], rider=rider)


def _out_proj_rmsnorm(mixed, w_out, x, norm_w, name):
    s, k = mixed.shape
    d = w_out.shape[1]
    tm = 512

    def body(a_ref, w_ref, x_ref, nw_ref, x1_ref, h2_ref):
        x1 = x_ref[...] + jnp.dot(a_ref[...], w_ref[...], preferred_element_type=F32)
        x1_ref[...] = x1
        rstd = lax.rsqrt(jnp.mean(x1 * x1, axis=-1, keepdims=True) + EPS)
        h2_ref[...] = (x1 * rstd * nw_ref[...]).astype(BF16)

    row = pl.BlockSpec((tm, d), lambda i: (i, 0))
    return pl.pallas_call(
        body, name=name, grid=(s // tm,),
        in_specs=[pl.BlockSpec((tm, k), lambda i: (i, 0)), pl.BlockSpec((k, d), lambda i: (0, 0)), row,
                  pl.BlockSpec((1, d), lambda i: (0, 0))],
        out_specs=[row, row],
        out_shape=[jax.ShapeDtypeStruct((s, d), F32), jax.ShapeDtypeStruct((s, d), BF16)],
        compiler_params=_params(),
    )(mixed, w_out, x, norm_w)


def _ffn_out_loss(act, w_ffn_out, x1, target, name):
    s, k = act.shape
    d = w_ffn_out.shape[1]
    tm = 512

    def body(a_ref, w_ref, x1_ref, t_ref, dy_ref, dyb_ref, loss_ref, acc):
        y = x1_ref[...] + jnp.dot(a_ref[...], w_ref[...], preferred_element_type=F32)
        diff = y - t_ref[...]
        dy = diff * (1.0 / d)
        dy_ref[...] = dy
        dyb_ref[...] = dy.astype(BF16)
        part = jnp.sum((diff * diff).reshape(tm // 8, 8, d), axis=0)
        i = pl.program_id(0)

        @pl.when(i == 0)
        def _():
            acc[...] = part

        @pl.when(i > 0)
        def _():
            acc[...] += part

        @pl.when(i == pl.num_programs(0) - 1)
        def _():
            loss_ref[...] = (0.5 / d) * jnp.sum(jnp.sum(acc[...], axis=1, keepdims=True), axis=0, keepdims=True)

    row = pl.BlockSpec((tm, d), lambda i: (i, 0))
    return pl.pallas_call(
        body, name=name, grid=(s // tm,),
        in_specs=[pl.BlockSpec((tm, k), lambda i: (i, 0)), pl.BlockSpec((k, d), lambda i: (0, 0)), row, row],
        out_specs=[row, row, pl.BlockSpec((1, 1), lambda i: (0, 0))],
        out_shape=[jax.ShapeDtypeStruct((s, d), F32), jax.ShapeDtypeStruct((s, d), BF16),
                   jax.ShapeDtypeStruct((1, 1), F32)],
        scratch_shapes=[pltpu.VMEM((8, d), F32)], compiler_params=_params(),
    )(act, w_ffn_out, x1, target)


LATE_GATHER = ("w_o_attn", "w_pw_conv", "w_out", "w_ffn_in", "w_ffn_out")
EARLY_REDUCE = LATE_GATHER


def _blocks_by_half(g):
    if g.ndim == 2:
        g = g.reshape(N_CHIPS, g.shape[0] // N_CHIPS, g.shape[1])
    return g.reshape(N_CHIPS, 2, g.shape[1] // 2, g.shape[2])


def _forward_backward(x, pos_col, target, wts, late_bufs, pos_arr):
    wts = dict(wts)
    consts = _rope_consts()
    bd = consts[3]
    qw2 = jnp.tile(wts["q_norm_w"], (1, LANES // HEAD_DIM))
    kw2 = jnp.tile(wts["k_norm_w"], (1, LANES // HEAD_DIM))
    qkv_w = 3 * N_SLOT_HEADS * HEAD_DIM
    conv_col0 = 3 * qkv_w
    ch = wts["conv_w"].shape[1]
    gate_col0 = conv_col0 + 2 * ch

    h = _rmsnorm_fwd(x, wts["norm1_w"], "rms1_fwd")
    gathering, started = _split_start(_gather_ici_rider(late_bufs, []), "late_gather_start", after=[wts["w_in"]])
    proj = _matmul(h, wts["w_in"], mode="nn", tm=512, tn=1920, tk=1024, out_dtype=F32, name="mm_proj",
                   b_blocked=True, cols_outer=True, after=[started])
    qn, kn = _qk_fwd(proj, pos_col, qw2, kw2, consts, "qk_fwd")
    attn, lse, attn_b = _attn_fwd(qn, kn, proj, "attn_fwd")
    late_bufs, _ = _split_wait(gathering, after=[attn_b], name="late_gather_wait")
    (u0, uc), late_bufs = _conv_fwd(proj, wts["conv_w"], wts["conv_b"], conv_col0, "conv_fwd",
                                    rider=_gather_forward_rider(late_bufs))
    for n, buf in zip(LATE_GATHER, late_bufs):
        full = buf.reshape(N_CHIPS, -1, buf.shape[3])
        wts[n] = full.reshape(-1, full.shape[2]) if n in ROW_SHARDED else full
    y_a = _matmul(attn_b, wts["w_o_attn"], mode="nn", tm=1024, tn=256, tk=512, out_dtype=F32, name="mm_ya",
                  b_blocked=True)
    u3 = _ln_silu_fwd(uc, wts["conv_ln_w"], wts["conv_ln_b"], "ln_fwd")
    y_b = _matmul(u3, wts["w_pw_conv"], mode="nn", tm=1024, tn=256, tk=512, out_dtype=F32, name="mm_yb",
                  b_blocked=True)
    mixed = _gate_fwd(proj, wts["b_gate"], y_a, y_b, gate_col0, "gate_fwd")
    x1, h2 = _out_proj_rmsnorm(mixed, wts["w_out"], x, wts["norm2_w"], "mm_x1_rms2")
    gate, up, act = _ffn_in_swiglu(h2, wts["w_ffn_in"], "mm_gu_swiglu")
    dy, dy_b16, loss = _ffn_out_loss(act, wts["w_ffn_out"], x1, target, "mm_x2_loss")

    g = {}
    by_chip = {}

    def pair_add(n, blocks, received):
        return _add_own_half(blocks, received, pos_arr, f"grads_pair_add_{n}")

    g_ffn_out = _blocks_by_half(
        _matmul(act, dy_b16, mode="tn", tm=1408, tn=1024, tk=2048, out_dtype=F32, name="mm_dwffnout"))
    dgu, (received,) = _ffn_out_bwd_swiglu(dy_b16, wts["w_ffn_out"], gate, up, "mm_dact_swiglu_bwd",
                                           rider=_pair_exchange_rider([g_ffn_out], halved=True))
    to_send, own = pair_add("w_ffn_out", g_ffn_out, received)
    dh2, (by_chip["w_ffn_out"],) = _matmul(
        dgu, wts["w_ffn_in"], mode="nt", tm=1024, tn=1024, tk=1408, out_dtype=F32, name="mm_dh2", b_blocked=True,
        rider=_chip_exchange_rider([to_send], [own]))
    g_ffn_in = _blocks_by_half(_matmul(h2, dgu, mode="tn", tm=512, tn=1408, tk=2048, out_dtype=F32,
                                       name="mm_dwffnin", out_blocked=N_CHIPS, cols_outer=True))
    exchanging, started = _split_start(_pair_exchange_rider([g_ffn_in], halved=True), "grads_ffn_in_pair_start")
    dx1, dx1_b16, g["norm2_w"] = _rmsnorm_bwd(dh2, x1, wts["norm2_w"], dy, "rms2_bwd")
    g["w_out"] = _matmul(mixed, dx1_b16, mode="tn", tm=512, tn=1024, tk=2048, out_dtype=F32, name="mm_dwout",
                         after=[started])
    dgl, dy_a, dy_b, g["b_gate"] = _out_proj_bwd_gates(dx1_b16, wts["w_out"], proj, wts["b_gate"], y_a, y_b,
                                                       gate_col0, "mm_dmixed_gate_bwd")
    (received,), (g_ffn_in,) = _split_wait(exchanging, after=[dgl], name="grads_ffn_in_pair_wait")
    ffn_in_to_send, ffn_in_own = pair_add("w_ffn_in", g_ffn_in, received)
    dattn = _matmul(dy_a, wts["w_o_attn"], mode="nt", tm=1024, tn=512, tk=256, out_dtype=F32, name="mm_dattn",
                    b_blocked=True)
    g["w_o_attn"] = _matmul(attn_b, dy_a, mode="tn", tm=512, tn=256, tk=2048, out_dtype=F32, name="mm_dwo",
                            out_blocked=N_CHIPS)
    du3 = _matmul(dy_b, wts["w_pw_conv"], mode="nt", tm=1024, tn=512, tk=256, out_dtype=F32, name="mm_du3",
                  b_blocked=True)
    g["w_pw_conv"] = _matmul(u3, dy_b, mode="tn", tm=512, tn=256, tk=2048, out_dtype=F32, name="mm_dwpw",
                             out_blocked=N_CHIPS)
    duc, g["conv_ln_w"], g["conv_ln_b"] = _ln_silu_bwd(du3, uc, wts["conv_ln_w"], wts["conv_ln_b"], "ln_bwd")

    small3 = ("w_out", "w_o_attn", "w_pw_conv")
    g_small3 = [_blocks_by_half(g.pop(n)) for n in small3]
    (da, db, g["conv_w"], g["conv_b"]), received = _conv_bwd(
        duc, u0, proj, wts["conv_w"], conv_col0, "conv_bwd", rider=_pair_exchange_rider(g_small3, halved=True))
    sums3 = [pair_add(n, gb, rv) for n, gb, rv in zip(small3, g_small3, received)]
    (dqn, dkn, dv), (by_chip["w_ffn_in"],) = _attn_bwd(
        qn, kn, proj, dattn, attn, lse, bd, "attn_bwd",
        rider=_chip_exchange_rider([ffn_in_to_send], [ffn_in_own]))
    (dproj, dqw, dkw), exchanged3 = _qk_bwd(
        dqn, dkn, dv, da, db, dgl, proj, pos_col, qw2, kw2, consts, "qk_bwd",
        rider=_chip_exchange_rider([s[0] for s in sums3], [s[1] for s in sums3]))
    by_chip.update(zip(small3, exchanged3))
    halves = [_sum_chips(by_chip[n], pos_arr, f"grads_chip_sum_{n}") for n in EARLY_REDUCE]
    g["q_norm_w"] = dqw[:, :HEAD_DIM]
    g["k_norm_w"] = dkw[:, :HEAD_DIM]

    c = pos_arr[0]
    rh = h.shape[1] // 2
    h_sibling = lax.dynamic_slice_in_dim(h, (1 - c) * rh, rh, axis=1)
    h_own = lax.dynamic_slice_in_dim(h, c * rh, rh, axis=1)
    g_sibling, shards = _matmul(h_sibling, dproj, mode="tn", tm=rh, tn=1920, tk=2048, out_dtype=F32,
                                name="mm_dwin_sibling", out_blocked=N_CHIPS, rider=_pair_gather_rider(halves))
    reduced = dict(zip(EARLY_REDUCE, shards))
    exchanging, started = _split_start(_pair_exchange_rider([g_sibling], halved=False), "grads_w_in_pair_start")
    g_own = _matmul(h_own, dproj, mode="tn", tm=rh, tn=1920, tk=2048, out_dtype=F32, name="mm_dwin_own",
                    out_blocked=N_CHIPS, after=[started])
    (from_sibling,), _ = _split_wait(exchanging, after=[g_own], name="grads_w_in_pair_wait")
    to_send, own = _add_own_half(g_own, from_sibling, pos_arr, "grads_pair_add_w_in")
    in_flight, started = _split_start(_chip_exchange_rider([to_send], [own]), "grads_w_in_exchange_start")
    dh = _matmul(dproj, wts["w_in"], mode="nt", tm=1024, tn=1024, tk=1920, out_dtype=F32, name="mm_dh",
                 b_blocked=True, after=[started])
    grad_x, _, g["norm1_w"] = _rmsnorm_bwd(dh, x, wts["norm1_w"], dx1, "rms1_bwd")
    return loss, grad_x, g, reduced, (in_flight, started)


def _mesh_pos():
    return lax.axis_index("x"), lax.axis_index("y"), lax.axis_index("c")


def _other_chips(x, y):
    return [(1 - x, y), (x, 1 - y), (1 - x, 1 - y)]


def _cast_into_slot(shard, chip_arr, dtype, name, n_slots=N_CHIPS):
    r, c = shard.shape
    tr = r // 2 if r % 32 == 0 else r

    def body(chip_ref, s_ref, o_ref):
        del chip_ref
        o_ref[...] = s_ref[...].astype(dtype)

    return pl.pallas_call(
        body, name=name,
        grid_spec=pltpu.PrefetchScalarGridSpec(
            num_scalar_prefetch=1, grid=(r // tr,),
            in_specs=[pl.BlockSpec((tr, c), lambda i, chip_ref: (i, 0))],
            out_specs=pl.BlockSpec((None, tr, c), lambda i, chip_ref: (chip_ref[0], i, 0))),
        out_shape=jax.ShapeDtypeStruct((n_slots, r, c), dtype), compiler_params=_params(),
    )(chip_arr, shard)


GATHER_CHUNKS = 4


def _gather_both_legs_rider(big, small):
    nb = len(big)
    n = nb + len(small)
    nch = GATHER_CHUNKS

    def part(bufs, a, slot, half, ch):
        if a >= nb:
            return bufs[a].at[slot]
        rows = bufs[a].shape[2] // nch
        return bufs[a].at[slot, half, pl.ds(ch * rows, rows)]

    def pieces():
        return [(a, ch, k) for ch in range(nch) for a in range(n) for k in range(3) if a < nb or ch == 0]

    def ici(bufs, sems, a, ch, k, slot_of_src):
        x, y, c = _mesh_pos()
        px, py = _other_chips(x, y)[k]
        slot = 2 * x + y if slot_of_src == "mine" else 2 * px + py
        return pltpu.make_async_remote_copy(
            src_ref=part(bufs, a, slot, c, ch), dst_ref=part(bufs, a, slot, c, ch), send_sem=sems[0].at[a, ch, k],
            recv_sem=sems[1].at[a, ch, k], device_id=(px, py, c), device_id_type=MESH)

    def forward(bufs, sems, a, ch, k, half):
        x, y, c = _mesh_pos()
        px, py = _other_chips(x, y)[k]
        h = c if half == "mine" else 1 - c
        return pltpu.make_async_remote_copy(
            src_ref=part(bufs, a, 2 * px + py, h, ch), dst_ref=part(bufs, a, 2 * px + py, h, ch),
            send_sem=sems[2].at[a, ch, k], recv_sem=sems[3].at[a, ch, k], device_id=(x, y, 1 - c),
            device_id_type=MESH)

    def start(r_in, bufs, sems):
        for a, ch, k in pieces():
            ici(bufs, sems, a, ch, k, "mine").start()

    def wait(r_in, bufs, sems):
        for a, ch, k in pieces():
            ici(bufs, sems, a, ch, k, "theirs").wait_recv()
            if a < nb:
                forward(bufs, sems, a, ch, k, "mine").start()
        for a, ch, k in pieces():
            if a < nb:
                forward(bufs, sems, a, ch, k, "theirs").wait_recv()
        for a, ch, k in pieces():
            ici(bufs, sems, a, ch, k, "mine").wait_send()
            if a < nb:
                forward(bufs, sems, a, ch, k, "mine").wait_send()

    ops = list(big) + list(small)
    return _Rider(ops, [jax.ShapeDtypeStruct(o.shape, o.dtype) for o in ops], {i: i for i in range(n)},
                  [pltpu.SemaphoreType.DMA((n, nch, 3)), pltpu.SemaphoreType.DMA((n, nch, 3)),
                   pltpu.SemaphoreType.DMA((nb, nch, 3)), pltpu.SemaphoreType.DMA((nb, nch, 3))], start, wait)


def _comm_call(rider, name):
    def body():
        pass

    return _pallas(body, name=name, grid=(1,), in_specs=[], out_specs=[], out_shape=[], operands=[],
                   rider=rider)[1]


def _gather_ici_rider(big, small):
    nb = len(big)
    n = nb + len(small)

    def copies(bufs, sems):
        x, y, c = _mesh_pos()
        me = 2 * x + y
        part = lambda a, slot: bufs[a].at[slot, c] if a < nb else bufs[a].at[slot]
        out = []
        for a in range(n):
            for k, (px, py) in enumerate(_other_chips(x, y)):
                send = functools.partial(
                    pltpu.make_async_remote_copy,
                    src_ref=part(a, me), dst_ref=part(a, me), send_sem=sems[0].at[a, k],
                    recv_sem=sems[1].at[a, k], device_id=(px, py, c), device_id_type=MESH)
                recv = functools.partial(
                    pltpu.make_async_remote_copy,
                    src_ref=part(a, 2 * px + py), dst_ref=part(a, 2 * px + py), send_sem=sems[0].at[a, k],
                    recv_sem=sems[1].at[a, k], device_id=(px, py, c), device_id_type=MESH)
                out.append((send, recv))
        return out

    def start(r_in, r_out, sems):
        for send, _ in copies(r_out, sems):
            send().start()

    def wait(r_in, r_out, sems):
        cps = copies(r_out, sems)
        for _, recv in cps:
            recv().wait_recv()
        for send, _ in cps:
            send().wait_send()

    ops = list(big) + list(small)
    return _Rider(ops, [jax.ShapeDtypeStruct(o.shape, o.dtype) for o in ops], {i: i for i in range(n)},
                  [pltpu.SemaphoreType.DMA((n, 3)), pltpu.SemaphoreType.DMA((n, 3))], start, wait)


def _gather_forward_rider(big):
    n = len(big)

    def copies(bufs, sems):
        x, y, c = _mesh_pos()
        out = []
        for a in range(n):
            for k, (px, py) in enumerate(_other_chips(x, y)):
                slot = 2 * px + py
                send = functools.partial(
                    pltpu.make_async_remote_copy,
                    src_ref=bufs[a].at[slot, c], dst_ref=bufs[a].at[slot, c], send_sem=sems[0].at[a, k],
                    recv_sem=sems[1].at[a, k], device_id=(x, y, 1 - c), device_id_type=MESH)
                recv = functools.partial(
                    pltpu.make_async_remote_copy,
                    src_ref=bufs[a].at[slot, 1 - c], dst_ref=bufs[a].at[slot, 1 - c], send_sem=sems[0].at[a, k],
                    recv_sem=sems[1].at[a, k], device_id=(x, y, 1 - c), device_id_type=MESH)
                out.append((send, recv))
        return out

    def start(r_in, r_out, sems):
        for send, _ in copies(r_out, sems):
            send().start()

    def wait(r_in, r_out, sems):
        cps = copies(r_out, sems)
        for _, recv in cps:
            recv().wait_recv()
        for send, _ in cps:
            send().wait_send()

    return _Rider(big, [jax.ShapeDtypeStruct(o.shape, o.dtype) for o in big], {i: i for i in range(n)},
                  [pltpu.SemaphoreType.DMA((n, 3)), pltpu.SemaphoreType.DMA((n, 3))], start, wait)


def _pair_exchange_rider(gs, halved):
    n = len(gs)

    def copies(r_in, r_out, sems):
        x, y, c = _mesh_pos()
        return [pltpu.make_async_remote_copy(
            src_ref=r_in[a].at[:, 1 - c] if halved else r_in[a], dst_ref=r_out[a], send_sem=sems[0].at[a],
            recv_sem=sems[1].at[a], device_id=(x, y, 1 - c), device_id_type=MESH) for a in range(n)]

    def start(r_in, r_out, sems):
        for cp in copies(r_in, r_out, sems):
            cp.start()

    def wait(r_in, r_out, sems):
        for cp in copies(r_in, r_out, sems):
            cp.wait()

    return _Rider(gs, [jax.ShapeDtypeStruct((g.shape[0],) + g.shape[-2:], g.dtype) for g in gs], {},
                  [pltpu.SemaphoreType.DMA((n,)), pltpu.SemaphoreType.DMA((n,))], start, wait)


def _chip_exchange_rider(to_send, by_chip, row_range=None):
    n = len(to_send)

    def copies(r_in, r_out, sems):
        x, y, c = _mesh_pos()
        me = 2 * x + y
        rows = (lambda ref: ref) if row_range is None else (lambda ref: ref.at[pl.ds(*row_range)])
        out = []
        for a in range(n):
            for k, (px, py) in enumerate(_other_chips(x, y)):
                send = functools.partial(
                    pltpu.make_async_remote_copy,
                    src_ref=rows(r_in[a].at[2 * px + py]), dst_ref=rows(r_out[a].at[me]),
                    send_sem=sems[0].at[a, k], recv_sem=sems[1].at[a, k], device_id=(px, py, c),
                    device_id_type=MESH)
                recv = functools.partial(
                    pltpu.make_async_remote_copy,
                    src_ref=rows(r_in[a].at[me]), dst_ref=rows(r_out[a].at[2 * px + py]),
                    send_sem=sems[0].at[a, k], recv_sem=sems[1].at[a, k], device_id=(px, py, c),
                    device_id_type=MESH)
                out.append((send, recv))
        return out

    def start(r_in, r_out, sems):
        for send, _ in copies(r_in, r_out, sems):
            send().start()

    def wait(r_in, r_out, sems):
        cps = copies(r_in, r_out, sems)
        for _, recv in cps:
            recv().wait_recv()
        for send, _ in cps:
            send().wait_send()

    return _Rider(list(to_send) + list(by_chip), [jax.ShapeDtypeStruct(b.shape, b.dtype) for b in by_chip],
                  {n + i: i for i in range(n)},
                  [pltpu.SemaphoreType.DMA((n, 3)), pltpu.SemaphoreType.DMA((n, 3))], start, wait)


HBM = pl.BlockSpec(memory_space=pltpu.HBM)
SEM = pl.BlockSpec(memory_space=pltpu.SEMAPHORE)


_IN_FLIGHT = pltpu.CompilerParams(has_side_effects=pltpu.SideEffectType.DATAFLOW_SIDE_EFFECTING)


class _FlatSems:
    def __init__(self, ref, shape):
        self.ref, self.shape = ref, shape

    @property
    def at(self):
        return self

    def __getitem__(self, idx):
        idx = idx if isinstance(idx, tuple) else (idx,)
        flat = 0
        for i, n in zip(idx, self.shape):
            flat = flat * n + i
        return self.ref.at[flat]


def _flat_sem_types(rider):
    return tuple(pltpu.SemaphoreType.DMA((int(np.prod(s.shape)),)) for s in rider.scratch)


def _as_rider_sems(rider, refs):
    return [_FlatSems(r, s.shape) for r, s in zip(refs, rider.scratch)]


def _split_start(rider, name, after=()):
    n_in, n_out, n_sem = len(rider.operands), len(rider.out_shapes), len(rider.scratch)
    n_after = len(after)
    fresh = [j for j in range(n_out) if j not in rider.aliases.values()]
    by_out = {j: i for i, j in rider.aliases.items()}

    def body(*refs):
        r_in = refs[:n_in]
        refs = refs[n_in + n_after:]
        sems = refs[:n_sem]
        thru = refs[n_sem:n_sem + n_in]
        fresh_refs = refs[n_sem + n_in:n_sem + n_in + len(fresh)]
        token = refs[-1]
        r_out = [thru[by_out[j]] if j in by_out else fresh_refs[fresh.index(j)] for j in range(n_out)]
        rider.start(r_in, r_out, _as_rider_sems(rider, sems))
        token[...] = jnp.zeros_like(token)

    res = pl.pallas_call(
        body, name=name,
        out_shape=_flat_sem_types(rider) + tuple(pltpu.HBM(o.shape, o.dtype) for o in rider.operands)
        + tuple(pltpu.HBM(rider.out_shapes[j].shape, rider.out_shapes[j].dtype) for j in fresh)
        + (jax.ShapeDtypeStruct((8, LANES), F32),),
        in_specs=(HBM,) * n_in + (ANY,) * n_after,
        out_specs=(SEM,) * n_sem + (HBM,) * (n_in + len(fresh)) + (pl.BlockSpec(memory_space=pltpu.VMEM),),
        input_output_aliases={i: n_sem + i for i in range(n_in)}, compiler_params=_IN_FLIGHT,
    )(*[pltpu.with_memory_space_constraint(o, pltpu.HBM) for o in rider.operands], *after)
    return (rider, res[:n_sem], res[n_sem:n_sem + n_in], res[n_sem + n_in:-1]), res[-1]


def _split_wait(handles, after, name):
    rider, sems, thru, fresh_arrays = handles
    n_in, n_out, n_sem = len(rider.operands), len(rider.out_shapes), len(rider.scratch)
    fresh = [j for j in range(n_out) if j not in rider.aliases.values()]
    by_out = {j: i for i, j in rider.aliases.items()}
    n_data = n_in + len(fresh)

    def body(*refs):
        r_in = refs[:n_in]
        fresh_refs = refs[n_in:n_data]
        sem_refs = refs[n_data:n_data + n_sem]
        r_out = [r_in[by_out[j]] if j in by_out else fresh_refs[fresh.index(j)] for j in range(n_out)]
        rider.wait(r_in, r_out, _as_rider_sems(rider, sem_refs))

    data = list(thru) + list(fresh_arrays)
    res = pl.pallas_call(
        body, name=name, out_shape=tuple(pltpu.HBM(d.shape, d.dtype) for d in data),
        in_specs=(HBM,) * n_data + (SEM,) * n_sem + (ANY,) * len(after), out_specs=(HBM,) * n_data,
        input_output_aliases={i: i for i in range(n_data)}, compiler_params=_IN_FLIGHT,
    )(*data, *sems, *after)
    return [res[by_out[j]] if j in by_out else res[n_in + fresh.index(j)] for j in range(n_out)], res[:n_in]


def _pair_gather_rider(bufs):
    n = len(bufs)

    def copies(r_out, sems):
        x, y, c = _mesh_pos()
        out = []
        for a in range(n):
            send = functools.partial(
                    pltpu.make_async_remote_copy,
                src_ref=r_out[a].at[c], dst_ref=r_out[a].at[c], send_sem=sems[0].at[a],
                recv_sem=sems[1].at[a], device_id=(x, y, 1 - c), device_id_type=MESH)
            recv = functools.partial(
                    pltpu.make_async_remote_copy,
                src_ref=r_out[a].at[1 - c], dst_ref=r_out[a].at[1 - c], send_sem=sems[0].at[a],
                recv_sem=sems[1].at[a], device_id=(x, y, 1 - c), device_id_type=MESH)
            out.append((send, recv))
        return out

    def start(r_in, r_out, sems):
        for send, _ in copies(r_out, sems):
            send().start()

    def wait(r_in, r_out, sems):
        cps = copies(r_out, sems)
        for _, recv in cps:
            recv().wait_recv()
        for send, _ in cps:
            send().wait_send()

    return _Rider(bufs, [jax.ShapeDtypeStruct(b.shape, b.dtype) for b in bufs], {i: i for i in range(n)},
                  [pltpu.SemaphoreType.DMA((n,)), pltpu.SemaphoreType.DMA((n,))], start, wait)


def _add_own_half(g, recv, pos_arr, name):
    nb, rh, cols = g.shape[0], g.shape[-2], g.shape[-1]

    def body(pos_ref, g_ref, r_ref, send_ref, own_ref):
        s = (g_ref[...] + r_ref[...]).astype(BF16)
        send_ref[...] = s

        @pl.when(pl.program_id(0) == pos_ref[1])
        def _():
            own_ref[...] = s

    blk = pl.BlockSpec((None, rh, cols), lambda j, pos_ref: (j, 0, 0))
    g_spec = blk if g.ndim == 3 else pl.BlockSpec((None, None, rh, cols),
                                                   lambda j, pos_ref: (j, pos_ref[0], 0, 0))
    shape = jax.ShapeDtypeStruct((nb, rh, cols), BF16)
    return pl.pallas_call(
        body, name=name,
        grid_spec=pltpu.PrefetchScalarGridSpec(
            num_scalar_prefetch=1, grid=(nb,), in_specs=[g_spec, blk],
            out_specs=[blk, pl.BlockSpec((None, rh, cols), lambda j, pos_ref: (pos_ref[1], 0, 0))]),
        out_shape=[shape, shape], compiler_params=_params(),
    )(pos_arr, g, recv)


def _sum_chips(gath, pos_arr, name):
    nb, rh, cols = gath.shape

    def body(pos_ref, a_ref, b_ref, c_ref, d_ref, o_ref):
        del pos_ref
        o_ref[...] = ((a_ref[...].astype(F32) + b_ref[...].astype(F32)) + c_ref[...].astype(F32)) \
            + d_ref[...].astype(F32)

    tr = rh // 2 if (rh // 2) % 16 == 0 else rh
    specs = [pl.BlockSpec((None, tr, cols), functools.partial(lambda i, pos_ref, j: (j, i, 0), j=j))
             for j in range(nb)]
    return pl.pallas_call(
        body, name=name,
        grid_spec=pltpu.PrefetchScalarGridSpec(
            num_scalar_prefetch=1, grid=(rh // tr,), in_specs=specs,
            out_specs=pl.BlockSpec((None, tr, cols), lambda i, pos_ref: (pos_ref[0], i, 0))),
        out_shape=jax.ShapeDtypeStruct((2, rh, cols), F32), compiler_params=_params(),
    )(pos_arr, gath, gath, gath, gath)


N_DEVICES = 8


def _small_gather_rider(buf):
    def copies(r_out, sems):
        x, y, c = _mesh_pos()
        me = 4 * x + 2 * y + c
        out = []
        for r in range(1, N_DEVICES):
            px = 1 - x if r & 4 else x
            py = 1 - y if r & 2 else y
            pc = 1 - c if r & 1 else c
            out.append(pltpu.make_async_remote_copy(
                src_ref=r_out[0].at[me], dst_ref=r_out[0].at[me], send_sem=sems[0].at[r - 1],
                recv_sem=sems[1].at[r - 1], device_id=(px, py, pc), device_id_type=MESH))
        return out

    def start(r_in, r_out, sems):
        for cp in copies(r_out, sems):
            cp.start()

    def wait(r_in, r_out, sems):
        cps = copies(r_out, sems)
        for cp in cps:
            cp.wait_recv()
        for cp in cps:
            cp.wait_send()

    return _Rider([buf], [jax.ShapeDtypeStruct(buf.shape, buf.dtype)], {0: 0},
                  [pltpu.SemaphoreType.DMA((N_DEVICES - 1,)), pltpu.SemaphoreType.DMA((N_DEVICES - 1,))],
                  start, wait)


def _sum_devices(buf, name):
    def body(b_ref, o_ref):
        acc = b_ref[0]
        for i in range(1, N_DEVICES):
            acc = acc + b_ref[i]
        o_ref[...] = acc

    return _pallas(body, name=name, grid=(1,), in_specs=[pl.BlockSpec(buf.shape, lambda i: (0, 0, 0))],
                   out_specs=pl.BlockSpec(buf.shape[1:], lambda i: (0, 0)),
                   out_shape=jax.ShapeDtypeStruct(buf.shape[1:], F32), operands=[buf])


def _adamw_math(w, g, m, v):
    m = ADAM_B1 * m + (1.0 - ADAM_B1) * g
    v = ADAM_B2 * v + (1.0 - ADAM_B2) * (g * g)
    m_hat = m / (1.0 - ADAM_B1 ** ADAM_STEP)
    v_hat = v / (1.0 - ADAM_B2 ** ADAM_STEP)
    delta = -ADAM_LR * (m_hat / (jnp.sqrt(v_hat) + ADAM_EPS) + ADAM_WD * w)
    return delta, m, v


def _adamw(w, g, m, v, name, after=(), rider=None):
    r, c = w.shape
    tr = 128 if r % 128 == 0 else 64
    assert r % tr == 0

    def body(w_ref, g_ref, m_ref, v_ref, go_ref, d_ref, mo_ref, vo_ref):
        gv = g_ref[...]
        d, mn, vn = _adamw_math(w_ref[...], gv, m_ref[...], v_ref[...])
        go_ref[...] = gv
        d_ref[...] = d
        mo_ref[...] = mn
        vo_ref[...] = vn

    blk = pl.BlockSpec((tr, c), lambda i: (i, 0))
    return _pallas(body, name=name, grid=(r // tr,), in_specs=[blk] * 4, out_specs=[blk] * 4,
                   out_shape=[jax.ShapeDtypeStruct((r, c), F32)] * 4, operands=[w, g, m, v], after=after,
                   rider=rider)


def _adamw_small(ws, gs, ms, vs, name):
    n = len(ws)

    def body(*refs):
        w_r, g_r, m_r, v_r = refs[:n], refs[n:2 * n], refs[2 * n:3 * n], refs[3 * n:4 * n]
        d_o, m_o, v_o = refs[4 * n:5 * n], refs[5 * n:6 * n], refs[6 * n:7 * n]
        for i in range(n):
            d, mn, vn = _adamw_math(w_r[i][...], g_r[i][...], m_r[i][...], v_r[i][...])
            d_o[i][...] = d
            m_o[i][...] = mn
            v_o[i][...] = vn

    specs = [pl.BlockSpec(w.shape, lambda i: (0, 0)) for w in ws]
    shapes = [jax.ShapeDtypeStruct(w.shape, F32) for w in ws]
    outs = pl.pallas_call(
        body, name=name, grid=(1,), in_specs=specs * 4, out_specs=specs * 3, out_shape=shapes * 3,
        compiler_params=_params(),
    )(*ws, *gs, *ms, *vs)
    return outs[:n], outs[n:2 * n], outs[2 * n:]


BIG = ("w_in", "w_o_attn", "w_pw_conv", "w_out", "w_ffn_in", "w_ffn_out")
ROW_SHARDED = ("w_out", "w_ffn_out")
SMALL = ("norm1_w", "b_gate", "q_norm_w", "k_norm_w", "conv_w", "conv_b", "conv_ln_w", "conv_ln_b", "norm2_w")
ORDER = ("norm1_w", "w_in", "b_gate", "q_norm_w", "k_norm_w", "w_o_attn", "conv_w", "conv_b", "conv_ln_w",
         "conv_ln_b", "w_pw_conv", "w_out", "norm2_w", "w_ffn_in", "w_ffn_out")
PACK_TILE = 8 * LANES


def _pack_small(parts):
    rows = []
    for p in parts:
        flat = p.reshape(-1)
        pad = (-flat.shape[0]) % PACK_TILE
        rows.append(jnp.pad(flat, (0, pad)).reshape(-1, LANES))
    return jnp.concatenate(rows, axis=0)


def _unpack_small(packed, shapes):
    out, row = [], 0
    for shp in shapes:
        size = int(np.prod(shp))
        nrow = -(-size // PACK_TILE) * (PACK_TILE // LANES)
        out.append(packed[row:row + nrow].reshape(-1)[:size].reshape(shp))
        row += nrow
    return out


def kernel(x, positions, norm1_w, w_in, b_gate, q_norm_w, k_norm_w, w_o_attn, conv_w, conv_b, conv_ln_w, conv_ln_b, w_pw_conv, w_out, norm2_w, w_ffn_in, w_ffn_out, loss_target, m_norm1_w, m_w_in, m_b_gate, m_q_norm_w, m_k_norm_w, m_w_o_attn, m_conv_w, m_conv_b, m_conv_ln_w, m_conv_ln_b, m_w_pw_conv, m_w_out, m_norm2_w, m_w_ffn_in, m_w_ffn_out, v_norm1_w, v_w_in, v_b_gate, v_q_norm_w, v_k_norm_w, v_w_o_attn, v_conv_w, v_conv_b, v_conv_ln_w, v_conv_ln_b, v_w_pw_conv, v_w_out, v_norm2_w, v_w_ffn_in, v_w_ffn_out):
    w = dict(norm1_w=norm1_w, w_in=w_in, b_gate=b_gate, q_norm_w=q_norm_w, k_norm_w=k_norm_w, w_o_attn=w_o_attn,
             conv_w=conv_w, conv_b=conv_b, conv_ln_w=conv_ln_w, conv_ln_b=conv_ln_b, w_pw_conv=w_pw_conv,
             w_out=w_out, norm2_w=norm2_w, w_ffn_in=w_ffn_in, w_ffn_out=w_ffn_out)
    m = dict(norm1_w=m_norm1_w, w_in=m_w_in, b_gate=m_b_gate, q_norm_w=m_q_norm_w, k_norm_w=m_k_norm_w,
             w_o_attn=m_w_o_attn, conv_w=m_conv_w, conv_b=m_conv_b, conv_ln_w=m_conv_ln_w,
             conv_ln_b=m_conv_ln_b, w_pw_conv=m_w_pw_conv, w_out=m_w_out, norm2_w=m_norm2_w,
             w_ffn_in=m_w_ffn_in, w_ffn_out=m_w_ffn_out)
    v = dict(norm1_w=v_norm1_w, w_in=v_w_in, b_gate=v_b_gate, q_norm_w=v_q_norm_w, k_norm_w=v_k_norm_w,
             w_o_attn=v_w_o_attn, conv_w=v_conv_w, conv_b=v_conv_b, conv_ln_w=v_conv_ln_w,
             conv_ln_b=v_conv_ln_b, w_pw_conv=v_w_pw_conv, w_out=v_w_out, norm2_w=v_norm2_w,
             w_ffn_in=v_w_ffn_in, w_ffn_out=v_w_ffn_out)
    cx, cy, cc = _mesh_pos()
    chip = 2 * cx + cy

    chip_arr = chip.reshape(1).astype(jnp.int32)
    pos_arr = jnp.stack([cc, chip]).astype(jnp.int32)
    bufs = {}
    for n in BIG:
        buf = _cast_into_slot(w[n][0], chip_arr, BF16, f"cast_{n}")
        bufs[n] = buf.reshape(N_CHIPS, 2, buf.shape[1] // 2, buf.shape[2])
    small_bufs = [_cast_into_slot(w[n][0], chip_arr, F32, f"slot_{n}") for n in ("conv_w", "b_gate")]
    w_in_buf, conv_w_buf, b_gate_buf = _comm_call(_gather_both_legs_rider([bufs["w_in"]], small_bufs),
                                                  "allgather_w_in")
    wts = dict(w_in=w_in_buf.reshape(N_CHIPS, -1, w_in_buf.shape[3]),
               conv_w=conv_w_buf.transpose(1, 0, 2).reshape(CONV_WIDTH, -1),
               b_gate=b_gate_buf.transpose(1, 0, 2).reshape(2, 1, -1),
               norm1_w=norm1_w, q_norm_w=q_norm_w, k_norm_w=k_norm_w, conv_b=conv_b, conv_ln_w=conv_ln_w,
               conv_ln_b=conv_ln_b, norm2_w=norm2_w)

    loss, grad_x, g, reduced, w_in_in_flight = _forward_backward(
        x[0], positions.reshape(-1, 1), loss_target[0], wts, [bufs[n] for n in LATE_GATHER], pos_arr)
    grads = {n: b.reshape(-1, b.shape[2]) for n, b in reduced.items()}

    w_in_in_flight, started = w_in_in_flight
    delta, new_m, new_v = {}, {}, {}
    carrier = "w_ffn_in"
    for n in EARLY_REDUCE:
        if n != carrier:
            grads[n], delta[n], new_m[n], new_v[n] = _adamw(w[n][0], grads[n], m[n][0], v[n][0], f"adamw_{n}",
                                                            after=[started])
    small_parts = [loss] + [g[n] for n in SMALL]
    small_shapes = [p.shape for p in small_parts]
    device_arr = (4 * cx + 2 * cy + cc).reshape(1).astype(jnp.int32)
    small_buf = _cast_into_slot(_pack_small(small_parts), device_arr, F32, "slot_small", n_slots=N_DEVICES)
    n = carrier
    (grads[n], delta[n], new_m[n], new_v[n]), (small_buf,) = _adamw(
        w[n][0], grads[n], m[n][0], v[n][0], f"adamw_{n}", rider=_small_gather_rider(small_buf),
        after=[started] + [delta[k] for k in EARLY_REDUCE if k != carrier])
    summed = _sum_devices(small_buf, "small_sum")
    reduced = _unpack_small(summed, small_shapes)
    loss_total = reduced[0].reshape(())
    for n, r in zip(SMALL, reduced[1:]):
        grads[n] = r
    ch_shard = conv_w.shape[2]
    grads["conv_w"] = lax.dynamic_slice_in_dim(grads["conv_w"], chip * ch_shard, ch_shard, axis=1)
    d_shard = b_gate.shape[2]
    grads["b_gate"] = lax.dynamic_slice_in_dim(grads["b_gate"], chip * d_shard, d_shard, axis=1)

    (by_chip_w_in,), _ = _split_wait(w_in_in_flight, after=[delta[n] for n in EARLY_REDUCE] + [summed],
                                     name="grads_w_in_exchange_wait")
    half_w_in = _sum_chips(by_chip_w_in, pos_arr, "grads_chip_sum_w_in")
    (shard_w_in,) = _comm_call(_pair_gather_rider([half_w_in]), "grads_pair_gather_w_in")
    grads["w_in"], delta["w_in"], new_m["w_in"], new_v["w_in"] = _adamw(
        w["w_in"][0], shard_w_in.reshape(-1, shard_w_in.shape[2]), m["w_in"][0], v["w_in"][0], "adamw_w_in")
    flat2 = lambda a: a.reshape(-1, a.shape[-1])
    d_s, m_s, v_s = _adamw_small([flat2(w[n]) for n in SMALL], [flat2(grads[n]) for n in SMALL],
                                 [flat2(m[n]) for n in SMALL], [flat2(v[n]) for n in SMALL], "adamw_small")
    for i, n in enumerate(SMALL):
        delta[n], new_m[n], new_v[n] = d_s[i], m_s[i], v_s[i]

    shaped = lambda d, n: d[n].reshape(w[n].shape)
    return (loss_total, grad_x[None], *[shaped(grads, n) for n in ORDER], *[shaped(delta, n) for n in ORDER],
            *[shaped(new_m, n) for n in ORDER], *[shaped(new_v, n) for n in ORDER])
```

```python
import functools

import numpy as np
import jax
import jax.numpy as jnp
from jax import lax
from jax.experimental import pallas as pl
from jax.experimental.pallas import tpu as pltpu

F32 = jnp.float32
BF16 = jnp.bfloat16
MESH = pl.DeviceIdType.MESH
ANY = pl.BlockSpec(memory_space=pl.ANY)

HEAD_DIM = 64
N_SLOT_HEADS = 8
DILATIONS = (1, 4, 16)
HALF_SPAN = 64
ROPE_THETA = 500000.0
ROT_DIM = 16
CONV_WIDTH = 31
EPS = 1e-6
NEG_INF = -1e30
ADAM_LR, ADAM_B1, ADAM_B2, ADAM_EPS, ADAM_WD, ADAM_STEP = 0.001, 0.9, 0.999, 1e-08, 0.01, 10

LANES = 128
QBLK = 128
KWIN = QBLK + 2 * HALF_SPAN
VMEM_LIMIT = 48 * 1024 * 1024
N_CHIPS = 4


def _params(**kw):
    return pltpu.CompilerParams(vmem_limit_bytes=VMEM_LIMIT, **kw)


class _Rider:
    def __init__(self, operands, out_shapes, aliases, scratch, start, wait):
        self.operands, self.out_shapes, self.aliases = list(operands), list(out_shapes), dict(aliases)
        self.scratch, self.start, self.wait = list(scratch), start, wait


def _riders_together(a, b):
    n_in, n_out, n_sc = len(a.operands), len(a.out_shapes), len(a.scratch)
    aliases = dict(a.aliases)
    aliases.update({n_in + src: n_out + dst for src, dst in b.aliases.items()})

    def start(r_in, r_out, r_sc):
        a.start(r_in[:n_in], r_out[:n_out], r_sc[:n_sc])
        b.start(r_in[n_in:], r_out[n_out:], r_sc[n_sc:])

    def wait(r_in, r_out, r_sc):
        a.wait(r_in[:n_in], r_out[:n_out], r_sc[:n_sc])
        b.wait(r_in[n_in:], r_out[n_out:], r_sc[n_sc:])

    return _Rider(a.operands + b.operands, a.out_shapes + b.out_shapes, aliases, a.scratch + b.scratch, start, wait)


def _pallas(body, *, name, grid, in_specs, out_specs, out_shape, operands, scratch_shapes=(), aliases=None,
            rider=None, after=()):
    single = not isinstance(out_specs, (list, tuple))
    out_specs_l = [out_specs] if single else list(out_specs)
    out_shape_l = [out_shape] if single else list(out_shape)
    aliases = dict(aliases or {})

    def call(fn, all_in_specs, all_out_specs, all_out_shape, all_scratch, all_aliases, all_operands):
        return pl.pallas_call(
            fn, name=name, grid=grid, in_specs=all_in_specs, out_specs=all_out_specs, out_shape=all_out_shape,
            scratch_shapes=all_scratch, input_output_aliases=all_aliases, compiler_params=_params(),
        )(*all_operands)

    if rider is None:
        n_main = len(in_specs)

        def ordered(*refs):
            body(*refs[:n_main], *refs[n_main + len(after):])

        res = call(ordered if after else body, list(in_specs) + [ANY] * len(after), out_specs_l, out_shape_l,
                   list(scratch_shapes), aliases, list(operands) + list(after))
        return res[0] if single else res
    assert not after
    n_in, n_rin = len(in_specs), len(rider.operands)
    n_out, n_rout = len(out_specs_l), len(rider.out_shapes)
    n_sc = len(scratch_shapes)

    def wrapped(*refs):
        main_in, r_in = refs[:n_in], refs[n_in:n_in + n_rin]
        o0 = n_in + n_rin
        main_out, r_out = refs[o0:o0 + n_out], refs[o0 + n_out:o0 + n_out + n_rout]
        s0 = o0 + n_out + n_rout
        main_sc, r_sc = refs[s0:s0 + n_sc], refs[s0 + n_sc:]
        ids = [pl.program_id(d) for d in range(len(grid))]
        first = functools.reduce(jnp.logical_and, [i == 0 for i in ids])
        last = functools.reduce(jnp.logical_and, [i == n - 1 for i, n in zip(ids, grid)])

        @pl.when(first)
        def _():
            rider.start(r_in, r_out, r_sc)

        body(*main_in, *main_out, *main_sc)

        @pl.when(last)
        def _():
            rider.wait(r_in, r_out, r_sc)

    for src, dst in rider.aliases.items():
        aliases[n_in + src] = n_out + dst
    res = call(wrapped, list(in_specs) + [ANY] * n_rin, out_specs_l + [ANY] * n_rout,
               out_shape_l + rider.out_shapes, list(scratch_shapes) + rider.scratch, aliases,
               list(operands) + rider.operands)
    main = res[:n_out]
    return (main[0] if single else main), res[n_out:]


def _matmul(a, b, *, mode, tm, tn, tk, out_dtype, name, b_blocked=False,
            out_blocked=None, cols_outer=False, rider=None, after=()):
    a_shape = a.shape
    if mode == "nn":
        m_dim, k_dim = a_shape
        n_dim = b.shape[0] * b.shape[2] if b_blocked else b.shape[1]
        rows, cols, red = m_dim, n_dim, k_dim
    elif mode == "nt":
        m_dim, n_dim = a_shape
        k_dim = b.shape[1] if b_blocked else b.shape[0]
        rows, cols, red = m_dim, k_dim, n_dim
    else:
        m_dim, k_dim = a_shape
        n_dim = b.shape[1]
        rows, cols, red = k_dim, n_dim, m_dim
    assert rows % tm == 0 and cols % tn == 0 and red % tk == 0, (name, rows, cols, red)
    ni, nj, nk = rows // tm, cols // tn, red // tk

    if mode == "nn":
        a_spec = pl.BlockSpec((tm, tk), lambda i, j, k: (i, k))
        if b_blocked:
            per = b.shape[2] // tn
            b_spec = pl.BlockSpec((None, tk, tn), lambda i, j, k: (j // per, k, j % per))
        else:
            b_spec = pl.BlockSpec((tk, tn), lambda i, j, k: (k, j))
        dims = (((1,), (0,)), ((), ()))
    elif mode == "nt":
        a_spec = pl.BlockSpec((tm, tk), lambda i, j, k: (i, k))
        if b_blocked:
            per = b.shape[2] // tk
            b_spec = pl.BlockSpec((None, tn, tk), lambda i, j, k: (k // per, j, k % per))
        else:
            b_spec = pl.BlockSpec((tn, tk), lambda i, j, k: (j, k))
        dims = (((1,), (1,)), ((), ()))
    else:
        a_spec = pl.BlockSpec((tk, tm), lambda i, j, k: (k, i))
        b_spec = pl.BlockSpec((tk, tn), lambda i, j, k: (k, j))
        dims = (((0,), (0,)), ((), ()))

    if out_blocked:
        per_o = (cols // out_blocked) // tn
        out_spec = pl.BlockSpec((None, tm, tn), lambda i, j, k: (j // per_o, i, j % per_o))
        out_shape = jax.ShapeDtypeStruct((out_blocked, rows, cols // out_blocked), out_dtype)
    else:
        out_spec = pl.BlockSpec((tm, tn), lambda i, j, k: (i, j))
        out_shape = jax.ShapeDtypeStruct((rows, cols), out_dtype)

    def body(a_ref, b_ref, o_ref, *acc):
        prod = lax.dot_general(a_ref[...], b_ref[...], dims, preferred_element_type=F32)
        if nk == 1:
            o_ref[...] = prod.astype(out_dtype)
        else:
            acc_ref, = acc
            k = pl.program_id(2)

            @pl.when(k == 0)
            def _():
                acc_ref[...] = prod

            @pl.when(k > 0)
            def _():
                acc_ref[...] += prod

            @pl.when(k == nk - 1)
            def _():
                o_ref[...] = acc_ref[...].astype(out_dtype)

    scratch = [pltpu.VMEM((tm, tn), F32)] if nk > 1 else []
    grid = (ni, nj, nk)
    if cols_outer:
        swap = lambda spec: pl.BlockSpec(spec.block_shape, lambda j, i, k, f=spec.index_map: f(i, j, k))
        a_spec, b_spec, out_spec, grid = swap(a_spec), swap(b_spec), swap(out_spec), (nj, ni, nk)
    return _pallas(body, name=name, grid=grid, in_specs=[a_spec, b_spec], out_specs=out_spec,
                   out_shape=out_shape, operands=[a, b], scratch_shapes=scratch, rider=rider, after=after)


def _rmsnorm_fwd(x, w, name):
    s, d = x.shape
    tm = 256

    def body(x_ref, w_ref, o_ref):
        xv = x_ref[...]
        rstd = lax.rsqrt(jnp.mean(xv * xv, axis=-1, keepdims=True) + EPS)
        o_ref[...] = (xv * rstd * w_ref[...]).astype(BF16)

    return pl.pallas_call(
        body, name=name, grid=(s // tm,),
        in_specs=[pl.BlockSpec((tm, d), lambda i: (i, 0)), pl.BlockSpec((1, d), lambda i: (0, 0))],
        out_specs=pl.BlockSpec((tm, d), lambda i: (i, 0)),
        out_shape=jax.ShapeDtypeStruct((s, d), BF16), compiler_params=_params(),
    )(x, w)


def _rmsnorm_bwd(dh, x, w, dres, name, rider=None):
    s, d = x.shape
    tm = 256

    def body(dh_ref, x_ref, w_ref, dres_ref, dx_ref, dxb_ref, dw_ref):
        xv = x_ref[...]
        rstd = lax.rsqrt(jnp.mean(xv * xv, axis=-1, keepdims=True) + EPS)
        xhat = xv * rstd
        dhv = dh_ref[...]
        g = dhv * w_ref[...]
        dx = rstd * (g - xhat * jnp.mean(g * xhat, axis=-1, keepdims=True)) + dres_ref[...]
        dx_ref[...] = dx
        dxb_ref[...] = dx.astype(BF16)
        part = jnp.sum(dhv * xhat, axis=0, keepdims=True)

        @pl.when(pl.program_id(0) == 0)
        def _():
            dw_ref[...] = part

        @pl.when(pl.program_id(0) > 0)
        def _():
            dw_ref[...] += part

    row = pl.BlockSpec((tm, d), lambda i: (i, 0))
    vec = pl.BlockSpec((1, d), lambda i: (0, 0))
    return _pallas(
        body, name=name, grid=(s // tm,), in_specs=[row, row, vec, row], out_specs=[row, row, vec],
        out_shape=[jax.ShapeDtypeStruct((s, d), F32), jax.ShapeDtypeStruct((s, d), BF16),
                   jax.ShapeDtypeStruct((1, d), F32)],
        operands=[dh, x, w, dres], rider=rider)


def _rope_consts():
    lane = np.arange(LANES)
    in_head = lane % HEAD_DIM
    inv_freq = ROPE_THETA ** (-jnp.arange(0, ROT_DIM, 2, dtype=F32) / ROT_DIM)
    invf = jnp.where(jnp.asarray(in_head < ROT_DIM), jnp.tile(inv_freq, LANES // (ROT_DIM // 2)), 0.0)
    m_a = np.where(in_head < ROT_DIM // 2, -1.0, 0.0).astype(np.float32)
    m_b = np.where((in_head >= ROT_DIM // 2) & (in_head < ROT_DIM), 1.0, 0.0).astype(np.float32)
    block_diag = (lane[:, None] // HEAD_DIM == lane[None, :] // HEAD_DIM).astype(np.float32)
    return (invf.reshape(1, LANES).astype(F32), jnp.asarray(m_a).reshape(1, LANES),
            jnp.asarray(m_b).reshape(1, LANES), jnp.asarray(block_diag, dtype=BF16))


def _head_sums(v, bd):
    hi = v.astype(BF16)
    lo = (v - hi.astype(F32)).astype(BF16)
    return jnp.dot(hi, bd, preferred_element_type=F32) + jnp.dot(lo, bd, preferred_element_type=F32)


def _qk_fwd(proj, pos_col, qw2, kw2, consts, name, rider=None):
    s = proj.shape[0]
    width = 3 * N_SLOT_HEADS * HEAD_DIM
    tm = 128
    invf, m_a, m_b, bd = consts
    scale = HEAD_DIM ** -0.5

    def body(q_ref, k_ref, pos_ref, qw_ref, kw_ref, invf_ref, ma_ref, mb_ref, bd_ref, qo_ref, ko_ref):
        ang = pos_ref[...].astype(F32) * invf_ref[...]
        cos = jnp.cos(ang)
        sin = jnp.sin(ang)
        s_a = sin * ma_ref[...]
        s_b = sin * mb_ref[...]
        bdv = bd_ref[...]
        for src, w_ref, dst, sc in ((q_ref, qw_ref, qo_ref, scale), (k_ref, kw_ref, ko_ref, 1.0)):
            for cb in range(width // LANES):
                cols = slice(cb * LANES, (cb + 1) * LANES)
                t = src[:, cols]
                rstd = lax.rsqrt(_head_sums(t * t, bdv) * (1.0 / HEAD_DIM) + EPS)
                y = t * rstd * w_ref[...]
                r = y * cos + pltpu.roll(y, LANES - 8, axis=1) * s_a + pltpu.roll(y, 8, axis=1) * s_b
                dst[:, cols] = r * sc if sc != 1.0 else r

    vec = pl.BlockSpec((1, LANES), lambda i: (0, 0))
    return _pallas(
        body, name=name, grid=(s // tm,),
        in_specs=[pl.BlockSpec((tm, width), lambda i: (i, 0)), pl.BlockSpec((tm, width), lambda i: (i, 1)),
                  pl.BlockSpec((tm, 1), lambda i: (i, 0)), vec, vec, vec, vec, vec,
                  pl.BlockSpec((LANES, LANES), lambda i: (0, 0))],
        out_specs=[pl.BlockSpec((tm, width), lambda i: (i, 0))] * 2,
        out_shape=[jax.ShapeDtypeStruct((s, width), F32)] * 2,
        operands=[proj, proj, pos_col, qw2, kw2, invf, m_a, m_b, bd], rider=rider)


def _qk_bwd(dqn, dkn, dv, da, db, dgl, proj, pos_col, qw2, kw2, consts, name, rider=None):
    s = proj.shape[0]
    width = 3 * N_SLOT_HEADS * HEAD_DIM
    ch = da.shape[1]
    gate_w = dgl.shape[1]
    out_w = 3 * width + 2 * ch + gate_w
    assert out_w == proj.shape[1]
    tm = 128
    invf, m_a, m_b, bd = consts
    scale = HEAD_DIM ** -0.5

    def body(dq_ref, dk_ref, dv_ref, da_ref, db_ref, dgl_ref, q_ref, k_ref, pos_ref, qw_ref, kw_ref,
             invf_ref, ma_ref, mb_ref, bd_ref, out_ref, dqw_ref, dkw_ref):
        ang = pos_ref[...].astype(F32) * invf_ref[...]
        cos = jnp.cos(ang)
        sin = jnp.sin(ang)
        s_a = sin * ma_ref[...]
        s_b = sin * mb_ref[...]
        bdv = bd_ref[...]
        first = pl.program_id(0) == 0
        for src, dsrc, w_ref, col0, dw_ref, sc in ((q_ref, dq_ref, qw_ref, 0, dqw_ref, scale),
                                                   (k_ref, dk_ref, kw_ref, width, dkw_ref, 1.0)):
            dw_acc = jnp.zeros((1, LANES), F32)
            for cb in range(width // LANES):
                cols = slice(cb * LANES, (cb + 1) * LANES)
                t = src[:, cols]
                dr = dsrc[:, cols]
                if sc != 1.0:
                    dr = dr * sc
                dy = dr * cos + pltpu.roll(dr * s_a, 8, axis=1) + pltpu.roll(dr * s_b, LANES - 8, axis=1)
                rstd = lax.rsqrt(_head_sums(t * t, bdv) * (1.0 / HEAD_DIM) + EPS)
                xhat = t * rstd
                g = dy * w_ref[...]
                dt = rstd * (g - xhat * (_head_sums(g * xhat, bdv) * (1.0 / HEAD_DIM)))
                out_ref[:, col0 + cb * LANES: col0 + (cb + 1) * LANES] = dt.astype(BF16)
                dw_acc = dw_acc + jnp.sum(dy * xhat, axis=0, keepdims=True)
            dw_acc = dw_acc + pltpu.roll(dw_acc, HEAD_DIM, axis=1)

            @pl.when(first)
            def _(dw_ref=dw_ref, dw_acc=dw_acc):
                dw_ref[...] = dw_acc

            @pl.when(jnp.logical_not(first))
            def _(dw_ref=dw_ref, dw_acc=dw_acc):
                dw_ref[...] += dw_acc
        out_ref[:, 2 * width: 3 * width] = dv_ref[...].astype(BF16)
        out_ref[:, 3 * width: 3 * width + ch] = da_ref[...]
        out_ref[:, 3 * width + ch: 3 * width + 2 * ch] = db_ref[...]
        out_ref[:, 3 * width + 2 * ch: out_w] = dgl_ref[...]

    vec = pl.BlockSpec((1, LANES), lambda i: (0, 0))
    blk = lambda c: pl.BlockSpec((tm, width), lambda i: (i, c))
    cblk = pl.BlockSpec((tm, ch), lambda i: (i, 0))
    return _pallas(
        body, name=name, grid=(s // tm,),
        in_specs=[blk(0), blk(0), blk(0), cblk, cblk, pl.BlockSpec((tm, gate_w), lambda i: (i, 0)),
                  blk(0), blk(1), pl.BlockSpec((tm, 1), lambda i: (i, 0)), vec, vec, vec, vec, vec,
                  pl.BlockSpec((LANES, LANES), lambda i: (0, 0))],
        out_specs=[pl.BlockSpec((tm, out_w), lambda i: (i, 0)), vec, vec],
        out_shape=[jax.ShapeDtypeStruct((s, out_w), BF16)] + [jax.ShapeDtypeStruct((1, LANES), F32)] * 2,
        operands=[dqn, dkn, dv, da, db, dgl, proj, proj, pos_col, qw2, kw2, invf, m_a, m_b, bd],
        rider=rider)


def _row_chunks(n_rows, fn, chunk=256):
    def step(i, c):
        fn(pl.ds(pl.multiple_of(i * chunk, chunk), chunk))
        return c
    lax.fori_loop(0, n_rows // chunk, step, 0)


def _to_residue_major(dst, src, s, d, dst_off=0, cast=None):
    seq = s // d
    for r in range(d):
        v = src[...] if d == 1 else src[pl.ds(r, seq, stride=d), :]
        dst[dst_off + r * seq: dst_off + (r + 1) * seq, :] = v if cast is None else v.astype(cast)


def _from_residue_major(dst, src, s, d, src_off=0):
    seq = s // d
    for r in range(d):
        v = src[src_off + r * seq: src_off + (r + 1) * seq, :]
        if d == 1:
            dst[...] = v
        else:
            dst[pl.ds(r, seq, stride=d), :] = v


def _band_bias():
    qi = lax.broadcasted_iota(jnp.int32, (QBLK, KWIN), 0)
    kj = lax.broadcasted_iota(jnp.int32, (QBLK, KWIN), 1)
    return jnp.where(jnp.abs(kj - HALF_SPAN - qi) <= HALF_SPAN, 0.0, NEG_INF).astype(F32)


def _range_bias(base, seq):
    kj = lax.broadcasted_iota(jnp.int32, (1, KWIN), 1)
    lo = (base & -seq) - base + HALF_SPAN
    return jnp.where((kj >= lo) & (kj < lo + seq), 0.0, NEG_INF).astype(F32)


def _skewed_blocks(n_blk, produce, consume):
    produce(0, 0)
    for b in range(n_blk):
        consume(b, b % 2)
        if b + 1 < n_blk:
            produce(b + 1, (b + 1) % 2)


def _block_base(b):
    return b * QBLK if isinstance(b, int) else pl.multiple_of(b * QBLK, QBLK)


def _attn_fwd(qn, kn, proj, name, rider=None):
    s = qn.shape[0]
    n_pairs = N_SLOT_HEADS * HEAD_DIM // LANES
    v_col0 = 2 * qn.shape[1] // LANES
    nt_dims = (((1,), (1,)), ((), ()))

    def body(q_ref, k_ref, v_ref, attn_ref, lse_ref, attn_b_ref, q_rm, k_rm, v_rm, acc_rm, m_rm, l_rm,
             acc_p, m_p, l_p, m_run, l_run, acc_run, band, s_buf, m_buf):
        g = pl.program_id(1)
        zpad = jnp.zeros((HALF_SPAN, LANES), BF16)
        k_rm[0:HALF_SPAN, :] = zpad
        k_rm[s + HALF_SPAN: s + 2 * HALF_SPAN, :] = zpad
        v_rm[0:HALF_SPAN, 0:LANES] = zpad
        v_rm[s + HALF_SPAN: s + 2 * HALF_SPAN, 0:LANES] = zpad

        def ones_rows(rows):
            v_rm[pl.ds(rows.start, rows.size), LANES:2 * LANES] = jnp.ones((rows.size, LANES), BF16)

        _row_chunks(s + 2 * HALF_SPAN, ones_rows, chunk=2 * HALF_SPAN)
        band[...] = _band_bias()
        lane = lax.broadcasted_iota(jnp.int32, (QBLK, LANES), 1)
        low = lane < HEAD_DIM
        n_blk = s // QBLK

        for gi, d in enumerate(DILATIONS):
            @pl.when(g == gi)
            def _(gi=gi, d=d):
                seq = s // d
                _to_residue_major(q_rm, q_ref, s, d, cast=BF16)
                _to_residue_major(k_rm, k_ref, s, d, dst_off=HALF_SPAN, cast=BF16)
                _to_residue_major(v_rm.at[:, 0:LANES], v_ref, s, d, dst_off=HALF_SPAN, cast=BF16)

                def scores(b, slot):
                    base = _block_base(b)
                    q = q_rm[pl.ds(base, QBLK), :]
                    zero = jnp.zeros_like(q)
                    q2 = jnp.concatenate([jnp.where(low, q, zero), jnp.where(low, zero, q)], axis=0)
                    sc = lax.dot_general(q2, k_rm[pl.ds(base, KWIN), :], nt_dims, preferred_element_type=F32)
                    bias = band[...] + _range_bias(base, seq)
                    for hh in range(2):
                        rows = slice(hh * QBLK, (hh + 1) * QBLK)
                        sh = sc[rows, :] + bias
                        s_buf[slot, rows, :] = sh
                        m_buf[slot, rows, :] = jnp.broadcast_to(jnp.max(sh, axis=-1, keepdims=True), (QBLK, LANES))

                def outputs(b, slot):
                    base = _block_base(b)
                    sv = s_buf[slot]
                    mb = m_buf[slot]
                    p = jnp.exp(jnp.concatenate([sv[:, 0:LANES] - mb, sv[:, LANES:2 * LANES] - mb], axis=1))
                    pv = jnp.dot(p.astype(BF16), v_rm[pl.ds(base, KWIN), :], preferred_element_type=F32)
                    rows = pl.ds(base, QBLK)
                    acc_rm[rows, :] = jnp.where(low, pv[0:QBLK, 0:LANES], pv[QBLK:2 * QBLK, 0:LANES])
                    l_rm[rows, :] = jnp.where(low, pv[0:QBLK, LANES:2 * LANES], pv[QBLK:2 * QBLK, LANES:2 * LANES])
                    m_rm[rows, :] = jnp.where(low, mb[0:QBLK, :], mb[QBLK:2 * QBLK, :])

                _skewed_blocks(n_blk, scores, outputs)
                if d == 1:
                    src = (acc_rm, m_rm, l_rm)
                else:
                    for dst_, src_ in ((acc_p, acc_rm), (m_p, m_rm), (l_p, l_rm)):
                        _from_residue_major(dst_, src_, s, d)
                    src = (acc_p, m_p, l_p)

                def combine(rows):
                    a_g, m_g, l_g = src[0][rows, :], src[1][rows, :], src[2][rows, :]
                    if gi == 0:
                        m_new, l_new, a_new = m_g, l_g, a_g
                    else:
                        m_old = m_run[rows, :]
                        m_new = jnp.maximum(m_old, m_g)
                        w_old = jnp.exp(m_old - m_new)
                        w_g = jnp.exp(m_g - m_new)
                        l_new = l_run[rows, :] * w_old + l_g * w_g
                        a_new = acc_run[rows, :] * w_old + a_g * w_g
                    if gi == len(DILATIONS) - 1:
                        out = a_new / l_new
                        attn_ref[rows, :] = out
                        attn_b_ref[rows, :] = out.astype(BF16)
                        lse_ref[rows, :] = m_new + jnp.log(l_new)
                    else:
                        m_run[rows, :] = m_new
                        l_run[rows, :] = l_new
                        acc_run[rows, :] = a_new

                _row_chunks(s, combine)

    qk_spec = pl.BlockSpec((s, LANES), lambda hp, g: (0, g * n_pairs + hp))
    v_spec = pl.BlockSpec((s, LANES), lambda hp, g: (0, v_col0 + g * n_pairs + hp))
    o_spec = pl.BlockSpec((s, LANES), lambda hp, g: (0, hp))
    f32buf = pltpu.VMEM((s, LANES), F32)
    return _pallas(
        body, name=name, grid=(n_pairs, len(DILATIONS)), in_specs=[qk_spec, qk_spec, v_spec],
        out_specs=[o_spec, o_spec, o_spec],
        out_shape=[jax.ShapeDtypeStruct((s, n_pairs * LANES), F32)] * 2
        + [jax.ShapeDtypeStruct((s, n_pairs * LANES), BF16)],
        operands=[qn, kn, proj],
        scratch_shapes=[pltpu.VMEM((s, LANES), BF16), pltpu.VMEM((s + 2 * HALF_SPAN, LANES), BF16),
                        pltpu.VMEM((s + 2 * HALF_SPAN, 2 * LANES), BF16)] + [f32buf] * 9
        + [pltpu.VMEM((QBLK, KWIN), F32), pltpu.VMEM((2, 2 * QBLK, KWIN), F32),
           pltpu.VMEM((2, 2 * QBLK, LANES), F32)],
        rider=rider)


def _attn_bwd(qn, kn, proj, dattn, attn, lse, bd, name, rider=None):
    s = qn.shape[0]
    n_pairs = N_SLOT_HEADS * HEAD_DIM // LANES
    v_col0 = 2 * qn.shape[1] // LANES
    nt_dims = (((1,), (1,)), ((), ()))
    tn_dims = (((0,), (0,)), ((), ()))
    spad = s + 2 * HALF_SPAN

    def body(q_ref, k_ref, v_ref, do_ref, o_ref, lse_ref, bd_ref, dq_ref, dk_ref, dv_ref,
             q_rm, k_rm, v_rm, do_rm, lse0_rm, lse1_rm, dd0_rm, dd1_rm, dq_rm, dk_rm, dv_rm,
             lse0_p, lse1_p, dd0_p, dd1_p, band, p_buf, ds_buf):
        g = pl.program_id(1)
        zpad = jnp.zeros((HALF_SPAN, LANES), BF16)
        for buf in (k_rm, v_rm):
            buf[0:HALF_SPAN, :] = zpad
            buf[s + HALF_SPAN: spad, :] = zpad
        zf = jnp.zeros((HALF_SPAN, LANES), F32)
        for buf in (dk_rm, dv_rm):
            buf[0:HALF_SPAN, :] = zf
            buf[s + HALF_SPAN: spad, :] = zf
        band[...] = _band_bias()

        def clear(rows):
            z = jnp.zeros((rows.size, LANES), F32)
            dk_rm[pl.ds(rows.start + HALF_SPAN, rows.size), :] = z
            dv_rm[pl.ds(rows.start + HALF_SPAN, rows.size), :] = z

        _row_chunks(s, clear)

        def prepare(rows):
            lo = lax.broadcasted_iota(jnp.int32, (rows.size, LANES), 1) < HEAD_DIM
            dsum = _head_sums(do_ref[rows, :] * o_ref[rows, :], bd_ref[...])
            dswap = pltpu.roll(dsum, HEAD_DIM, axis=1)
            dd0_p[rows, :] = jnp.where(lo, dsum, dswap)
            dd1_p[rows, :] = jnp.where(lo, dswap, dsum)
            lv = lse_ref[rows, :]
            lswap = pltpu.roll(lv, HEAD_DIM, axis=1)
            lse0_p[rows, :] = jnp.where(lo, lv, lswap)
            lse1_p[rows, :] = jnp.where(lo, lswap, lv)

        @pl.when(g == 0)
        def _():
            _row_chunks(s, prepare)
        lane = lax.broadcasted_iota(jnp.int32, (QBLK, LANES), 1)
        low = lane < HEAD_DIM
        n_blk = s // QBLK

        def stacked(ref, rows):
            val = ref[rows, :]
            zero = jnp.zeros_like(val)
            return jnp.concatenate([jnp.where(low, val, zero), jnp.where(low, zero, val)], axis=0)

        for gi, d in enumerate(DILATIONS):
            @pl.when(g == gi)
            def _(d=d):
                seq = s // d
                _to_residue_major(q_rm, q_ref, s, d, cast=BF16)
                _to_residue_major(k_rm, k_ref, s, d, dst_off=HALF_SPAN, cast=BF16)
                _to_residue_major(v_rm, v_ref, s, d, dst_off=HALF_SPAN, cast=BF16)
                _to_residue_major(do_rm, do_ref, s, d, cast=BF16)
                for dst_, src_ in ((lse0_rm, lse0_p), (lse1_rm, lse1_p), (dd0_rm, dd0_p), (dd1_rm, dd1_p)):
                    _to_residue_major(dst_, src_, s, d)

                def scores(b, slot):
                    base = _block_base(b)
                    rows = pl.ds(base, QBLK)
                    win = pl.ds(base, KWIN)
                    sc = lax.dot_general(stacked(q_rm, rows), k_rm[win, :], nt_dims, preferred_element_type=F32)
                    dp = lax.dot_general(stacked(do_rm, rows), v_rm[win, :], nt_dims, preferred_element_type=F32)
                    bias = band[...] + _range_bias(base, seq)
                    for hh, (lse_r, dd_r) in enumerate(((lse0_rm, dd0_rm), (lse1_rm, dd1_rm))):
                        r = slice(hh * QBLK, (hh + 1) * QBLK)
                        lse_h = lse_r[rows, :]
                        dd_h = dd_r[rows, :]
                        sh = sc[r, :] + bias
                        p = jnp.exp(jnp.concatenate([sh[:, 0:LANES] - lse_h, sh[:, LANES:KWIN] - lse_h], axis=1))
                        dph = dp[r, :]
                        ds = p * jnp.concatenate([dph[:, 0:LANES] - dd_h, dph[:, LANES:KWIN] - dd_h], axis=1)
                        p_buf[slot, r, :] = p.astype(BF16)
                        ds_buf[slot, r, :] = ds.astype(BF16)

                def grads(b, slot):
                    base = _block_base(b)
                    rows = pl.ds(base, QBLK)
                    win = pl.ds(base, KWIN)
                    p = p_buf[slot]
                    ds = ds_buf[slot]
                    dq2 = jnp.dot(ds, k_rm[win, :], preferred_element_type=F32)
                    dq_rm[rows, :] = jnp.where(low, dq2[0:QBLK, :], dq2[QBLK:2 * QBLK, :])
                    dk_rm[win, :] += lax.dot_general(ds, stacked(q_rm, rows), tn_dims, preferred_element_type=F32)
                    dv_rm[win, :] += lax.dot_general(p, stacked(do_rm, rows), tn_dims, preferred_element_type=F32)

                _skewed_blocks(n_blk, scores, grads)
                _from_residue_major(dq_ref, dq_rm, s, d)
                _from_residue_major(dk_ref, dk_rm, s, d, src_off=HALF_SPAN)
                _from_residue_major(dv_ref, dv_rm, s, d, src_off=HALF_SPAN)

    qk_spec = pl.BlockSpec((s, LANES), lambda hp, g: (0, g * n_pairs + hp))
    v_spec = pl.BlockSpec((s, LANES), lambda hp, g: (0, v_col0 + g * n_pairs + hp))
    o_spec = pl.BlockSpec((s, LANES), lambda hp, g: (0, hp))
    width = qn.shape[1]
    f32buf = pltpu.VMEM((s, LANES), F32)
    f32pad = pltpu.VMEM((spad, LANES), F32)
    return _pallas(
        body, name=name, grid=(n_pairs, len(DILATIONS)),
        in_specs=[qk_spec, qk_spec, v_spec, o_spec, o_spec, o_spec,
                  pl.BlockSpec((LANES, LANES), lambda hp, g: (0, 0))],
        out_specs=[qk_spec, qk_spec, qk_spec],
        out_shape=[jax.ShapeDtypeStruct((s, width), F32)] * 3,
        operands=[qn, kn, proj, dattn, attn, lse, bd],
        scratch_shapes=[pltpu.VMEM((s, LANES), BF16), pltpu.VMEM((spad, LANES), BF16),
                        pltpu.VMEM((spad, LANES), BF16), pltpu.VMEM((s, LANES), BF16),
                        f32buf, f32buf, f32buf, f32buf, f32buf, f32pad, f32pad,
                        f32buf, f32buf, f32buf, f32buf, pltpu.VMEM((QBLK, KWIN), F32),
                        pltpu.VMEM((2, 2 * QBLK, KWIN), BF16), pltpu.VMEM((2, 2 * QBLK, KWIN), BF16)],
        rider=rider)


CONV_PAD = 16


def _conv_fwd(proj, conv_w, conv_b, col0, name, rider=None):
    s = proj.shape[0]
    ch = conv_w.shape[1]
    nblk = ch // LANES
    a0 = col0 // LANES
    tr = 256
    shift = CONV_PAD - (CONV_WIDTH - 1) // 2

    def body(a_ref, b_ref, w_ref, bias_ref, u0_ref, uc_ref, pad):
        z = jnp.zeros((CONV_PAD, LANES), F32)
        pad[0:CONV_PAD, :] = z
        pad[s + CONV_PAD: s + 2 * CONV_PAD, :] = z

        def glu(rows):
            u0 = a_ref[rows, :] * jax.nn.sigmoid(b_ref[rows, :])
            u0_ref[rows, :] = u0
            pad[pl.ds(rows.start + CONV_PAD, rows.size), :] = u0

        _row_chunks(s, glu)
        for t in range(0, s, tr):
            acc = jnp.broadcast_to(bias_ref[...], (tr, LANES))
            for k in range(CONV_WIDTH):
                acc = acc + w_ref[k:k + 1, :] * pad[t + k + shift: t + k + shift + tr, :]
            uc_ref[t:t + tr, :] = acc

    return _pallas(
        body, name=name, grid=(nblk,),
        in_specs=[pl.BlockSpec((s, LANES), lambda c: (0, a0 + c)),
                  pl.BlockSpec((s, LANES), lambda c: (0, a0 + nblk + c)),
                  pl.BlockSpec((CONV_WIDTH, LANES), lambda c: (0, c)),
                  pl.BlockSpec((1, LANES), lambda c: (0, c))],
        out_specs=[pl.BlockSpec((s, LANES), lambda c: (0, c))] * 2,
        out_shape=[jax.ShapeDtypeStruct((s, ch), F32)] * 2, operands=[proj, proj, conv_w, conv_b],
        scratch_shapes=[pltpu.VMEM((s + 2 * CONV_PAD, LANES), F32)], rider=rider)


def _ln_silu_fwd(uc, ln_w, ln_b, name):
    s, ch = uc.shape
    tm = 256

    def body(u_ref, w_ref, b_ref, o_ref):
        u = u_ref[...]
        mu = jnp.mean(u, axis=-1, keepdims=True)
        xc = u - mu
        rstd = lax.rsqrt(jnp.mean(xc * xc, axis=-1, keepdims=True) + EPS)
        z = xc * rstd * w_ref[...] + b_ref[...]
        o_ref[...] = (z * jax.nn.sigmoid(z)).astype(BF16)

    row = pl.BlockSpec((tm, ch), lambda i: (i, 0))
    vec = pl.BlockSpec((1, ch), lambda i: (0, 0))
    return pl.pallas_call(
        body, name=name, grid=(s // tm,), in_specs=[row, vec, vec], out_specs=row,
        out_shape=jax.ShapeDtypeStruct((s, ch), BF16), compiler_params=_params(),
    )(uc, ln_w, ln_b)


def _ln_silu_bwd(du3, uc, ln_w, ln_b, name):
    s, ch = uc.shape
    tm = 256

    def body(d_ref, u_ref, w_ref, b_ref, du_ref, dw_ref, db_ref):
        u = u_ref[...]
        mu = jnp.mean(u, axis=-1, keepdims=True)
        xc = u - mu
        rstd = lax.rsqrt(jnp.mean(xc * xc, axis=-1, keepdims=True) + EPS)
        xhat = xc * rstd
        z = xhat * w_ref[...] + b_ref[...]
        sg = jax.nn.sigmoid(z)
        dz = d_ref[...] * (sg * (1.0 + z * (1.0 - sg)))
        dxh = dz * w_ref[...]
        du_ref[...] = rstd * (dxh - jnp.mean(dxh, axis=-1, keepdims=True)
                              - xhat * jnp.mean(dxh * xhat, axis=-1, keepdims=True))
        pw = jnp.sum(dz * xhat, axis=0, keepdims=True)
        pb = jnp.sum(dz, axis=0, keepdims=True)
        first = pl.program_id(0) == 0

        @pl.when(first)
        def _():
            dw_ref[...] = pw
            db_ref[...] = pb

        @pl.when(jnp.logical_not(first))
        def _():
            dw_ref[...] += pw
            db_ref[...] += pb

    row = pl.BlockSpec((tm, ch), lambda i: (i, 0))
    vec = pl.BlockSpec((1, ch), lambda i: (0, 0))
    return pl.pallas_call(
        body, name=name, grid=(s // tm,), in_specs=[row, row, vec, vec], out_specs=[row, vec, vec],
        out_shape=[jax.ShapeDtypeStruct((s, ch), F32), jax.ShapeDtypeStruct((1, ch), F32),
                   jax.ShapeDtypeStruct((1, ch), F32)],
        compiler_params=_params(),
    )(du3, uc, ln_w, ln_b)


def _conv_bwd(duc, u0, proj, conv_w, col0, name, rider=None):
    s = proj.shape[0]
    ch = conv_w.shape[1]
    nblk = ch // LANES
    a0 = col0 // LANES
    tr = 256
    half = (CONV_WIDTH - 1) // 2
    shift = CONV_PAD - half

    def body(duc_ref, u0_ref, a_ref, b_ref, w_ref, da_ref, db_ref, dw_ref, dbias_ref, pad_d, pad_u):
        z = jnp.zeros((CONV_PAD, LANES), F32)
        for buf in (pad_d, pad_u):
            buf[0:CONV_PAD, :] = z
            buf[s + CONV_PAD: s + 2 * CONV_PAD, :] = z

        def fill(rows):
            dst = pl.ds(rows.start + CONV_PAD, rows.size)
            pad_d[dst, :] = duc_ref[rows, :]
            pad_u[dst, :] = u0_ref[rows, :]

        _row_chunks(s, fill)
        dw_acc = [jnp.zeros((8, LANES), F32) for _ in range(CONV_WIDTH)]
        dbias_acc = jnp.zeros((8, LANES), F32)
        for t in range(0, s, tr):
            d_t = duc_ref[t:t + tr, :]
            dbias_acc = dbias_acc + jnp.sum(d_t.reshape(tr // 8, 8, LANES), axis=0)
            du0 = jnp.zeros((tr, LANES), F32)
            for k in range(CONV_WIDTH):
                du0 = du0 + w_ref[k:k + 1, :] * pad_d[t - k + half + CONV_PAD: t - k + half + CONV_PAD + tr, :]
                prod = d_t * pad_u[t + k + shift: t + k + shift + tr, :]
                dw_acc[k] = dw_acc[k] + jnp.sum(prod.reshape(tr // 8, 8, LANES), axis=0)
            av = a_ref[t:t + tr, :]
            sg = jax.nn.sigmoid(b_ref[t:t + tr, :])
            da_ref[t:t + tr, :] = (du0 * sg).astype(BF16)
            db_ref[t:t + tr, :] = (du0 * av * sg * (1.0 - sg)).astype(BF16)
        for k in range(CONV_WIDTH):
            dw_ref[k:k + 1, :] = jnp.sum(dw_acc[k], axis=0, keepdims=True)
        dbias_ref[...] = jnp.sum(dbias_acc, axis=0, keepdims=True)

    col = lambda off: pl.BlockSpec((s, LANES), lambda c: (0, off + c))
    return _pallas(
        body, name=name, grid=(nblk,),
        in_specs=[col(0), col(0), col(a0), col(a0 + nblk),
                  pl.BlockSpec((CONV_WIDTH, LANES), lambda c: (0, c))],
        out_specs=[col(0), col(0), pl.BlockSpec((CONV_WIDTH, LANES), lambda c: (0, c)),
                   pl.BlockSpec((1, LANES), lambda c: (0, c))],
        out_shape=[jax.ShapeDtypeStruct((s, ch), BF16)] * 2
        + [jax.ShapeDtypeStruct((CONV_WIDTH, ch), F32), jax.ShapeDtypeStruct((1, ch), F32)],
        operands=[duc, u0, proj, proj, conv_w],
        scratch_shapes=[pltpu.VMEM((s + 2 * CONV_PAD, LANES), F32)] * 2, rider=rider)


GATE_BLK = 512


def _gate_fwd(proj, bg, y_a, y_b, col0, name):
    s, d = y_a.shape
    tm = 256
    g0 = col0 // GATE_BLK
    nb = d // GATE_BLK

    def body(ga_ref, gb_ref, ba_ref, bb_ref, ya_ref, yb_ref, o_ref):
        g_a = jax.nn.sigmoid(ga_ref[...] + ba_ref[...])
        g_b = jax.nn.sigmoid(gb_ref[...] + bb_ref[...])
        o_ref[...] = (g_a * ya_ref[...] + g_b * yb_ref[...]).astype(BF16)

    act = pl.BlockSpec((tm, GATE_BLK), lambda i, j: (i, j))
    return pl.pallas_call(
        body, name=name, grid=(s // tm, nb),
        in_specs=[pl.BlockSpec((tm, GATE_BLK), lambda i, j: (i, g0 + j)),
                  pl.BlockSpec((tm, GATE_BLK), lambda i, j: (i, g0 + nb + j)),
                  pl.BlockSpec((None, 1, GATE_BLK), lambda i, j: (0, 0, j)),
                  pl.BlockSpec((None, 1, GATE_BLK), lambda i, j: (1, 0, j)), act, act],
        out_specs=act, out_shape=jax.ShapeDtypeStruct((s, d), BF16), compiler_params=_params(),
    )(proj, proj, bg, bg, y_a, y_b)


def _out_proj_bwd_gates(dx1, w_out, proj, bg, y_a, y_b, col0, name, after=()):
    s, d = y_a.shape
    tm = 256
    half = d // 2
    assert col0 % half == 0
    c0 = col0 // half
    nt_dims = (((1,), (1,)), ((), ()))

    def body(dx_ref, w_ref, a0_ref, a1_ref, b0_ref, b1_ref, bias_ref, ya_ref, yb_ref,
             dgl_ref, dya_ref, dyb_ref, db_ref):
        dm = lax.dot_general(dx_ref[...], w_ref[...], nt_dims, preferred_element_type=F32)
        parts = []
        for br, (lo_ref, hi_ref, y_ref, dy_ref) in enumerate(((a0_ref, a1_ref, ya_ref, dya_ref),
                                                              (b0_ref, b1_ref, yb_ref, dyb_ref))):
            logits = jnp.concatenate([lo_ref[...], hi_ref[...]], axis=1)
            gate = jax.nn.sigmoid(logits + bias_ref[br])
            dy_ref[...] = (dm * gate).astype(BF16)
            dgl = dm * y_ref[...] * gate * (1.0 - gate)
            dgl_ref[:, br * d:(br + 1) * d] = dgl.astype(BF16)
            parts.append(jnp.sum(dgl, axis=0, keepdims=True))
        part = jnp.concatenate(parts, axis=0)
        first = pl.program_id(0) == 0

        @pl.when(first)
        def _():
            db_ref[...] = part

        @pl.when(jnp.logical_not(first))
        def _():
            db_ref[...] += part

    row = pl.BlockSpec((tm, d), lambda i: (i, 0))
    logit_blk = lambda k: pl.BlockSpec((tm, half), functools.partial(lambda i, k: (i, c0 + k), k=k))
    return _pallas(
        body, name=name, grid=(s // tm,),
        in_specs=[row, pl.BlockSpec((d, d), lambda i: (0, 0)), logit_blk(0), logit_blk(1), logit_blk(2),
                  logit_blk(3), pl.BlockSpec((2, 1, d), lambda i: (0, 0, 0)), row, row],
        out_specs=[pl.BlockSpec((tm, 2 * d), lambda i: (i, 0)), row, row, pl.BlockSpec((2, d), lambda i: (0, 0))],
        out_shape=[jax.ShapeDtypeStruct((s, 2 * d), BF16), jax.ShapeDtypeStruct((s, d), BF16),
                   jax.ShapeDtypeStruct((s, d), BF16), jax.ShapeDtypeStruct((2, d), F32)],
        operands=[dx1, w_out, proj, proj, proj, proj, bg, y_a, y_b], after=after)


def _ffn_in_swiglu(h2, w_blocked, name):
    s, k = h2.shape
    nblk, _, tn = w_blocked.shape
    ff = nblk // 2 * tn
    tm = 512

    def body(a_ref, wg_ref, wu_ref, g_ref, u_ref, act_ref):
        a = a_ref[...]
        gt = jnp.dot(a, wg_ref[...], preferred_element_type=F32)
        up = jnp.dot(a, wu_ref[...], preferred_element_type=F32)
        g_ref[...] = gt
        u_ref[...] = up
        act_ref[...] = (gt * jax.nn.sigmoid(gt) * up).astype(BF16)

    out = pl.BlockSpec((tm, tn), lambda j, i: (i, j))
    return pl.pallas_call(
        body, name=name, grid=(nblk // 2, s // tm),
        in_specs=[pl.BlockSpec((tm, k), lambda j, i: (i, 0)),
                  pl.BlockSpec((None, k, tn), lambda j, i: (j, 0, 0)),
                  pl.BlockSpec((None, k, tn), lambda j, i: (nblk // 2 + j, 0, 0))],
        out_specs=[out, out, out],
        out_shape=[jax.ShapeDtypeStruct((s, ff), F32), jax.ShapeDtypeStruct((s, ff), F32),
                   jax.ShapeDtypeStruct((s, ff), BF16)],
        compiler_params=_params(),
    )(h2, w_blocked, w_blocked)


def _ffn_out_bwd_swiglu(dy, w_ffn_out, gate, up, name, rider=None):
    s, d = dy.shape
    ff = gate.shape[1]
    tm = 256
    nt_dims = (((1,), (1,)), ((), ()))

    def body(dy_ref, w_ref, g_ref, u_ref, o_ref):
        dv = lax.dot_general(dy_ref[...], w_ref[...], nt_dims, preferred_element_type=F32)
        gt = g_ref[...]
        sg = jax.nn.sigmoid(gt)
        o_ref[:, 0:ff] = (dv * u_ref[...] * (sg * (1.0 + gt * (1.0 - sg)))).astype(BF16)
        o_ref[:, ff:2 * ff] = (dv * gt * sg).astype(BF16)

    row = pl.BlockSpec((tm, ff), lambda i: (i, 0))
    return _pallas(
        body, name=name, grid=(s // tm,),
        in_specs=[pl.BlockSpec((tm, d), lambda i: (i, 0)), pl.BlockSpec((ff, d), lambda i: (0, 0)), row, row],
        out_specs=pl.BlockSpec((tm, 2 * ff), lambda i: (i, 0)),
        out_shape=jax.ShapeDtypeStruct((s, 2 * ff), BF16), operands=[dy, w_ffn_out, gate, up], rider=rider)


def _out_proj_rmsnorm(mixed, w_out, x, norm_w, name):
    s, k = mixed.shape
    d = w_out.shape[1]
    tm = 512

    def body(a_ref, w_ref, x_ref, nw_ref, x1_ref, h2_ref):
        x1 = x_ref[...] + jnp.dot(a_ref[...], w_ref[...], preferred_element_type=F32)
        x1_ref[...] = x1
        rstd = lax.rsqrt(jnp.mean(x1 * x1, axis=-1, keepdims=True) + EPS)
        h2_ref[...] = (x1 * rstd * nw_ref[...]).astype(BF16)

    row = pl.BlockSpec((tm, d), lambda i: (i, 0))
    return pl.pallas_call(
        body, name=name, grid=(s // tm,),
        in_specs=[pl.BlockSpec((tm, k), lambda i: (i, 0)), pl.BlockSpec((k, d), lambda i: (0, 0)), row,
                  pl.BlockSpec((1, d), lambda i: (0, 0))],
        out_specs=[row, row],
        out_shape=[jax.ShapeDtypeStruct((s, d), F32), jax.ShapeDtypeStruct((s, d), BF16)],
        compiler_params=_params(),
    )(mixed, w_out, x, norm_w)


def _ffn_out_loss(act, w_ffn_out, x1, target, name):
    s, k = act.shape
    d = w_ffn_out.shape[1]
    tm = 512

    def body(a_ref, w_ref, x1_ref, t_ref, dy_ref, dyb_ref, loss_ref, acc):
        y = x1_ref[...] + jnp.dot(a_ref[...], w_ref[...], preferred_element_type=F32)
        diff = y - t_ref[...]
        dy = diff * (1.0 / d)
        dy_ref[...] = dy
        dyb_ref[...] = dy.astype(BF16)
        part = jnp.sum((diff * diff).reshape(tm // 8, 8, d), axis=0)
        i = pl.program_id(0)

        @pl.when(i == 0)
        def _():
            acc[...] = part

        @pl.when(i > 0)
        def _():
            acc[...] += part

        @pl.when(i == pl.num_programs(0) - 1)
        def _():
            loss_ref[...] = (0.5 / d) * jnp.sum(jnp.sum(acc[...], axis=1, keepdims=True), axis=0, keepdims=True)

    row = pl.BlockSpec((tm, d), lambda i: (i, 0))
    return pl.pallas_call(
        body, name=name, grid=(s // tm,),
        in_specs=[pl.BlockSpec((tm, k), lambda i: (i, 0)), pl.BlockSpec((k, d), lambda i: (0, 0)), row, row],
        out_specs=[row, row, pl.BlockSpec((1, 1), lambda i: (0, 0))],
        out_shape=[jax.ShapeDtypeStruct((s, d), F32), jax.ShapeDtypeStruct((s, d), BF16),
                   jax.ShapeDtypeStruct((1, 1), F32)],
        scratch_shapes=[pltpu.VMEM((8, d), F32)], compiler_params=_params(),
    )(act, w_ffn_out, x1, target)


LATE_GATHER = ("w_o_attn", "w_pw_conv", "w_out", "w_ffn_in", "w_ffn_out")
EARLY_REDUCE = LATE_GATHER


def _blocks_by_half(g):
    if g.ndim == 2:
        g = g.reshape(N_CHIPS, g.shape[0] // N_CHIPS, g.shape[1])
    return g.reshape(N_CHIPS, 2, g.shape[1] // 2, g.shape[2])


def _forward_backward(x, pos_col, target, wts, late_bufs, pos_arr):
    wts = dict(wts)
    consts = _rope_consts()
    bd = consts[3]
    qw2 = jnp.tile(wts["q_norm_w"], (1, LANES // HEAD_DIM))
    kw2 = jnp.tile(wts["k_norm_w"], (1, LANES // HEAD_DIM))
    qkv_w = 3 * N_SLOT_HEADS * HEAD_DIM
    conv_col0 = 3 * qkv_w
    ch = wts["conv_w"].shape[1]
    gate_col0 = conv_col0 + 2 * ch

    h = _rmsnorm_fwd(x, wts["norm1_w"], "rms1_fwd")
    gathering, started = _split_start(_gather_ici_rider(late_bufs, []), "late_gather_start", after=[wts["w_in"]])
    proj = _matmul(h, wts["w_in"], mode="nn", tm=512, tn=1920, tk=1024, out_dtype=F32, name="mm_proj",
                   b_blocked=True, cols_outer=True, after=[started])
    qn, kn = _qk_fwd(proj, pos_col, qw2, kw2, consts, "qk_fwd")
    attn, lse, attn_b = _attn_fwd(qn, kn, proj, "attn_fwd")
    late_bufs, _ = _split_wait(gathering, after=[attn_b], name="late_gather_wait")
    (u0, uc), late_bufs = _conv_fwd(proj, wts["conv_w"], wts["conv_b"], conv_col0, "conv_fwd",
                                    rider=_gather_forward_rider(late_bufs))
    for n, buf in zip(LATE_GATHER, late_bufs):
        full = buf.reshape(N_CHIPS, -1, buf.shape[3])
        wts[n] = full.reshape(-1, full.shape[2]) if n in ROW_SHARDED else full
    y_a = _matmul(attn_b, wts["w_o_attn"], mode="nn", tm=1024, tn=256, tk=512, out_dtype=F32, name="mm_ya",
                  b_blocked=True)
    u3 = _ln_silu_fwd(uc, wts["conv_ln_w"], wts["conv_ln_b"], "ln_fwd")
    y_b = _matmul(u3, wts["w_pw_conv"], mode="nn", tm=1024, tn=256, tk=512, out_dtype=F32, name="mm_yb",
                  b_blocked=True)
    mixed = _gate_fwd(proj, wts["b_gate"], y_a, y_b, gate_col0, "gate_fwd")
    x1, h2 = _out_proj_rmsnorm(mixed, wts["w_out"], x, wts["norm2_w"], "mm_x1_rms2")
    gate, up, act = _ffn_in_swiglu(h2, wts["w_ffn_in"], "mm_gu_swiglu")
    dy, dy_b16, loss = _ffn_out_loss(act, wts["w_ffn_out"], x1, target, "mm_x2_loss")

    g = {}
    by_chip = {}

    def pair_add(n, blocks, received):
        return _add_own_half(blocks, received, pos_arr, f"grads_pair_add_{n}")

    g_ffn_out = _blocks_by_half(
        _matmul(act, dy_b16, mode="tn", tm=1408, tn=1024, tk=2048, out_dtype=F32, name="mm_dwffnout"))
    dgu, (received,) = _ffn_out_bwd_swiglu(dy_b16, wts["w_ffn_out"], gate, up, "mm_dact_swiglu_bwd",
                                           rider=_pair_exchange_rider([g_ffn_out], halved=True))
    to_send, own = pair_add("w_ffn_out", g_ffn_out, received)
    dh2, (by_chip["w_ffn_out"],) = _matmul(
        dgu, wts["w_ffn_in"], mode="nt", tm=1024, tn=1024, tk=1408, out_dtype=F32, name="mm_dh2", b_blocked=True,
        rider=_chip_exchange_rider([to_send], [own]))
    g_ffn_in = _blocks_by_half(_matmul(h2, dgu, mode="tn", tm=512, tn=1408, tk=2048, out_dtype=F32,
                                       name="mm_dwffnin", out_blocked=N_CHIPS, cols_outer=True))
    exchanging, started = _split_start(_pair_exchange_rider([g_ffn_in], halved=True), "grads_ffn_in_pair_start")
    dx1, dx1_b16, g["norm2_w"] = _rmsnorm_bwd(dh2, x1, wts["norm2_w"], dy, "rms2_bwd")
    g["w_out"] = _matmul(mixed, dx1_b16, mode="tn", tm=512, tn=1024, tk=2048, out_dtype=F32, name="mm_dwout",
                         after=[started])
    dgl, dy_a, dy_b, g["b_gate"] = _out_proj_bwd_gates(dx1_b16, wts["w_out"], proj, wts["b_gate"], y_a, y_b,
                                                       gate_col0, "mm_dmixed_gate_bwd")
    (received,), (g_ffn_in,) = _split_wait(exchanging, after=[dgl], name="grads_ffn_in_pair_wait")
    ffn_in_to_send, ffn_in_own = pair_add("w_ffn_in", g_ffn_in, received)
    dattn = _matmul(dy_a, wts["w_o_attn"], mode="nt", tm=1024, tn=512, tk=256, out_dtype=F32, name="mm_dattn",
                    b_blocked=True)
    g["w_o_attn"] = _matmul(attn_b, dy_a, mode="tn", tm=512, tn=256, tk=2048, out_dtype=F32, name="mm_dwo",
                            out_blocked=N_CHIPS)
    du3 = _matmul(dy_b, wts["w_pw_conv"], mode="nt", tm=1024, tn=512, tk=256, out_dtype=F32, name="mm_du3",
                  b_blocked=True)
    g["w_pw_conv"] = _matmul(u3, dy_b, mode="tn", tm=512, tn=256, tk=2048, out_dtype=F32, name="mm_dwpw",
                             out_blocked=N_CHIPS)
    duc, g["conv_ln_w"], g["conv_ln_b"] = _ln_silu_bwd(du3, uc, wts["conv_ln_w"], wts["conv_ln_b"], "ln_bwd")

    small3 = ("w_out", "w_o_attn", "w_pw_conv")
    g_small3 = [_blocks_by_half(g.pop(n)) for n in small3]
    (da, db, g["conv_w"], g["conv_b"]), received = _conv_bwd(
        duc, u0, proj, wts["conv_w"], conv_col0, "conv_bwd", rider=_pair_exchange_rider(g_small3, halved=True))
    sums3 = [pair_add(n, gb, rv) for n, gb, rv in zip(small3, g_small3, received)]
    (dqn, dkn, dv), (by_chip["w_ffn_in"],) = _attn_bwd(
        qn, kn, proj, dattn, attn, lse, bd, "attn_bwd",
        rider=_chip_exchange_rider([ffn_in_to_send], [ffn_in_own]))
    (dproj, dqw, dkw), exchanged3 = _qk_bwd(
        dqn, dkn, dv, da, db, dgl, proj, pos_col, qw2, kw2, consts, "qk_bwd",
        rider=_chip_exchange_rider([s[0] for s in sums3], [s[1] for s in sums3]))
    by_chip.update(zip(small3, exchanged3))
    halves = [_sum_chips(by_chip[n], pos_arr, f"grads_chip_sum_{n}") for n in EARLY_REDUCE]
    g["q_norm_w"] = dqw[:, :HEAD_DIM]
    g["k_norm_w"] = dkw[:, :HEAD_DIM]

    c = pos_arr[0]
    rh = h.shape[1] // 2
    h_sibling = lax.dynamic_slice_in_dim(h, (1 - c) * rh, rh, axis=1)
    h_own = lax.dynamic_slice_in_dim(h, c * rh, rh, axis=1)
    g_sibling, shards = _matmul(h_sibling, dproj, mode="tn", tm=rh, tn=1920, tk=2048, out_dtype=F32,
                                name="mm_dwin_sibling", out_blocked=N_CHIPS, rider=_pair_gather_rider(halves))
    reduced = dict(zip(EARLY_REDUCE, shards))
    exchanging, started = _split_start(_pair_exchange_rider([g_sibling], halved=False), "grads_w_in_pair_start")
    g_own = _matmul(h_own, dproj, mode="tn", tm=rh, tn=1920, tk=2048, out_dtype=F32, name="mm_dwin_own",
                    out_blocked=N_CHIPS, after=[started])
    (from_sibling,), _ = _split_wait(exchanging, after=[g_own], name="grads_w_in_pair_wait")
    to_send, own = _add_own_half(g_own, from_sibling, pos_arr, "grads_pair_add_w_in")
    in_flight, started = _split_start(_chip_exchange_rider([to_send], [own]), "grads_w_in_exchange_start")
    dh = _matmul(dproj, wts["w_in"], mode="nt", tm=1024, tn=1024, tk=1920, out_dtype=F32, name="mm_dh",
                 b_blocked=True, after=[started])
    grad_x, _, g["norm1_w"] = _rmsnorm_bwd(dh, x, wts["norm1_w"], dx1, "rms1_bwd")
    return loss, grad_x, g, reduced, (in_flight, started)


def _mesh_pos():
    return lax.axis_index("x"), lax.axis_index("y"), lax.axis_index("c")


def _other_chips(x, y):
    return [(1 - x, y), (x, 1 - y), (1 - x, 1 - y)]


def _cast_into_slot(shard, chip_arr, dtype, name, n_slots=N_CHIPS):
    r, c = shard.shape
    tr = r // 2 if r % 32 == 0 else r

    def body(chip_ref, s_ref, o_ref):
        del chip_ref
        o_ref[...] = s_ref[...].astype(dtype)

    return pl.pallas_call(
        body, name=name,
        grid_spec=pltpu.PrefetchScalarGridSpec(
            num_scalar_prefetch=1, grid=(r // tr,),
            in_specs=[pl.BlockSpec((tr, c), lambda i, chip_ref: (i, 0))],
            out_specs=pl.BlockSpec((None, tr, c), lambda i, chip_ref: (chip_ref[0], i, 0))),
        out_shape=jax.ShapeDtypeStruct((n_slots, r, c), dtype), compiler_params=_params(),
    )(chip_arr, shard)


GATHER_CHUNKS = 4


def _gather_both_legs_rider(big, small):
    nb = len(big)
    n = nb + len(small)
    nch = GATHER_CHUNKS

    def part(bufs, a, slot, half, ch):
        if a >= nb:
            return bufs[a].at[slot]
        rows = bufs[a].shape[2] // nch
        return bufs[a].at[slot, half, pl.ds(ch * rows, rows)]

    def pieces():
        return [(a, ch, k) for ch in range(nch) for a in range(n) for k in range(3) if a < nb or ch == 0]

    def ici(bufs, sems, a, ch, k, slot_of_src):
        x, y, c = _mesh_pos()
        px, py = _other_chips(x, y)[k]
        slot = 2 * x + y if slot_of_src == "mine" else 2 * px + py
        return pltpu.make_async_remote_copy(
            src_ref=part(bufs, a, slot, c, ch), dst_ref=part(bufs, a, slot, c, ch), send_sem=sems[0].at[a, ch, k],
            recv_sem=sems[1].at[a, ch, k], device_id=(px, py, c), device_id_type=MESH)

    def forward(bufs, sems, a, ch, k, half):
        x, y, c = _mesh_pos()
        px, py = _other_chips(x, y)[k]
        h = c if half == "mine" else 1 - c
        return pltpu.make_async_remote_copy(
            src_ref=part(bufs, a, 2 * px + py, h, ch), dst_ref=part(bufs, a, 2 * px + py, h, ch),
            send_sem=sems[2].at[a, ch, k], recv_sem=sems[3].at[a, ch, k], device_id=(x, y, 1 - c),
            device_id_type=MESH)

    def start(r_in, bufs, sems):
        for a, ch, k in pieces():
            ici(bufs, sems, a, ch, k, "mine").start()

    def wait(r_in, bufs, sems):
        for a, ch, k in pieces():
            ici(bufs, sems, a, ch, k, "theirs").wait_recv()
            if a < nb:
                forward(bufs, sems, a, ch, k, "mine").start()
        for a, ch, k in pieces():
            if a < nb:
                forward(bufs, sems, a, ch, k, "theirs").wait_recv()
        for a, ch, k in pieces():
            ici(bufs, sems, a, ch, k, "mine").wait_send()
            if a < nb:
                forward(bufs, sems, a, ch, k, "mine").wait_send()

    ops = list(big) + list(small)
    return _Rider(ops, [jax.ShapeDtypeStruct(o.shape, o.dtype) for o in ops], {i: i for i in range(n)},
                  [pltpu.SemaphoreType.DMA((n, nch, 3)), pltpu.SemaphoreType.DMA((n, nch, 3)),
                   pltpu.SemaphoreType.DMA((nb, nch, 3)), pltpu.SemaphoreType.DMA((nb, nch, 3))], start, wait)


def _comm_call(rider, name):
    def body():
        pass

    return _pallas(body, name=name, grid=(1,), in_specs=[], out_specs=[], out_shape=[], operands=[],
                   rider=rider)[1]


def _gather_ici_rider(big, small):
    nb = len(big)
    n = nb + len(small)

    def copies(bufs, sems):
        x, y, c = _mesh_pos()
        me = 2 * x + y
        part = lambda a, slot: bufs[a].at[slot, c] if a < nb else bufs[a].at[slot]
        out = []
        for a in range(n):
            for k, (px, py) in enumerate(_other_chips(x, y)):
                send = functools.partial(
                    pltpu.make_async_remote_copy,
                    src_ref=part(a, me), dst_ref=part(a, me), send_sem=sems[0].at[a, k],
                    recv_sem=sems[1].at[a, k], device_id=(px, py, c), device_id_type=MESH)
                recv = functools.partial(
                    pltpu.make_async_remote_copy,
                    src_ref=part(a, 2 * px + py), dst_ref=part(a, 2 * px + py), send_sem=sems[0].at[a, k],
                    recv_sem=sems[1].at[a, k], device_id=(px, py, c), device_id_type=MESH)
                out.append((send, recv))
        return out

    def start(r_in, r_out, sems):
        for send, _ in copies(r_out, sems):
            send().start()

    def wait(r_in, r_out, sems):
        cps = copies(r_out, sems)
        for _, recv in cps:
            recv().wait_recv()
        for send, _ in cps:
            send().wait_send()

    ops = list(big) + list(small)
    return _Rider(ops, [jax.ShapeDtypeStruct(o.shape, o.dtype) for o in ops], {i: i for i in range(n)},
                  [pltpu.SemaphoreType.DMA((n, 3)), pltpu.SemaphoreType.DMA((n, 3))], start, wait)


def _gather_forward_rider(big):
    n = len(big)

    def copies(bufs, sems):
        x, y, c = _mesh_pos()
        out = []
        for a in range(n):
            for k, (px, py) in enumerate(_other_chips(x, y)):
                slot = 2 * px + py
                send = functools.partial(
                    pltpu.make_async_remote_copy,
                    src_ref=bufs[a].at[slot, c], dst_ref=bufs[a].at[slot, c], send_sem=sems[0].at[a, k],
                    recv_sem=sems[1].at[a, k], device_id=(x, y, 1 - c), device_id_type=MESH)
                recv = functools.partial(
                    pltpu.make_async_remote_copy,
                    src_ref=bufs[a].at[slot, 1 - c], dst_ref=bufs[a].at[slot, 1 - c], send_sem=sems[0].at[a, k],
                    recv_sem=sems[1].at[a, k], device_id=(x, y, 1 - c), device_id_type=MESH)
                out.append((send, recv))
        return out

    def start(r_in, r_out, sems):
        for send, _ in copies(r_out, sems):
            send().start()

    def wait(r_in, r_out, sems):
        cps = copies(r_out, sems)
        for _, recv in cps:
            recv().wait_recv()
        for send, _ in cps:
            send().wait_send()

    return _Rider(big, [jax.ShapeDtypeStruct(o.shape, o.dtype) for o in big], {i: i for i in range(n)},
                  [pltpu.SemaphoreType.DMA((n, 3)), pltpu.SemaphoreType.DMA((n, 3))], start, wait)


def _pair_exchange_rider(gs, halved):
    n = len(gs)

    def copies(r_in, r_out, sems):
        x, y, c = _mesh_pos()
        return [pltpu.make_async_remote_copy(
            src_ref=r_in[a].at[:, 1 - c] if halved else r_in[a], dst_ref=r_out[a], send_sem=sems[0].at[a],
            recv_sem=sems[1].at[a], device_id=(x, y, 1 - c), device_id_type=MESH) for a in range(n)]

    def start(r_in, r_out, sems):
        for cp in copies(r_in, r_out, sems):
            cp.start()

    def wait(r_in, r_out, sems):
        for cp in copies(r_in, r_out, sems):
            cp.wait()

    return _Rider(gs, [jax.ShapeDtypeStruct((g.shape[0],) + g.shape[-2:], g.dtype) for g in gs], {},
                  [pltpu.SemaphoreType.DMA((n,)), pltpu.SemaphoreType.DMA((n,))], start, wait)


def _chip_exchange_rider(to_send, by_chip, row_range=None):
    n = len(to_send)

    def copies(r_in, r_out, sems):
        x, y, c = _mesh_pos()
        me = 2 * x + y
        rows = (lambda ref: ref) if row_range is None else (lambda ref: ref.at[pl.ds(*row_range)])
        out = []
        for a in range(n):
            for k, (px, py) in enumerate(_other_chips(x, y)):
                send = functools.partial(
                    pltpu.make_async_remote_copy,
                    src_ref=rows(r_in[a].at[2 * px + py]), dst_ref=rows(r_out[a].at[me]),
                    send_sem=sems[0].at[a, k], recv_sem=sems[1].at[a, k], device_id=(px, py, c),
                    device_id_type=MESH)
                recv = functools.partial(
                    pltpu.make_async_remote_copy,
                    src_ref=rows(r_in[a].at[me]), dst_ref=rows(r_out[a].at[2 * px + py]),
                    send_sem=sems[0].at[a, k], recv_sem=sems[1].at[a, k], device_id=(px, py, c),
                    device_id_type=MESH)
                out.append((send, recv))
        return out

    def start(r_in, r_out, sems):
        for send, _ in copies(r_in, r_out, sems):
            send().start()

    def wait(r_in, r_out, sems):
        cps = copies(r_in, r_out, sems)
        for _, recv in cps:
            recv().wait_recv()
        for send, _ in cps:
            send().wait_send()

    return _Rider(list(to_send) + list(by_chip), [jax.ShapeDtypeStruct(b.shape, b.dtype) for b in by_chip],
                  {n + i: i for i in range(n)},
                  [pltpu.SemaphoreType.DMA((n, 3)), pltpu.SemaphoreType.DMA((n, 3))], start, wait)


HBM = pl.BlockSpec(memory_space=pltpu.HBM)
SEM = pl.BlockSpec(memory_space=pltpu.SEMAPHORE)


_IN_FLIGHT = pltpu.CompilerParams(has_side_effects=pltpu.SideEffectType.DATAFLOW_SIDE_EFFECTING)


class _FlatSems:
    def __init__(self, ref, shape):
        self.ref, self.shape = ref, shape

    @property
    def at(self):
        return self

    def __getitem__(self, idx):
        idx = idx if isinstance(idx, tuple) else (idx,)
        flat = 0
        for i, n in zip(idx, self.shape):
            flat = flat * n + i
        return self.ref.at[flat]


def _flat_sem_types(rider):
    return tuple(pltpu.SemaphoreType.DMA((int(np.prod(s.shape)),)) for s in rider.scratch)


def _as_rider_sems(rider, refs):
    return [_FlatSems(r, s.shape) for r, s in zip(refs, rider.scratch)]


def _split_start(rider, name, after=()):
    n_in, n_out, n_sem = len(rider.operands), len(rider.out_shapes), len(rider.scratch)
    n_after = len(after)
    fresh = [j for j in range(n_out) if j not in rider.aliases.values()]
    by_out = {j: i for i, j in rider.aliases.items()}

    def body(*refs):
        r_in = refs[:n_in]
        refs = refs[n_in + n_after:]
        sems = refs[:n_sem]
        thru = refs[n_sem:n_sem + n_in]
        fresh_refs = refs[n_sem + n_in:n_sem + n_in + len(fresh)]
        token = refs[-1]
        r_out = [thru[by_out[j]] if j in by_out else fresh_refs[fresh.index(j)] for j in range(n_out)]
        rider.start(r_in, r_out, _as_rider_sems(rider, sems))
        token[...] = jnp.zeros_like(token)

    res = pl.pallas_call(
        body, name=name,
        out_shape=_flat_sem_types(rider) + tuple(pltpu.HBM(o.shape, o.dtype) for o in rider.operands)
        + tuple(pltpu.HBM(rider.out_shapes[j].shape, rider.out_shapes[j].dtype) for j in fresh)
        + (jax.ShapeDtypeStruct((8, LANES), F32),),
        in_specs=(HBM,) * n_in + (ANY,) * n_after,
        out_specs=(SEM,) * n_sem + (HBM,) * (n_in + len(fresh)) + (pl.BlockSpec(memory_space=pltpu.VMEM),),
        input_output_aliases={i: n_sem + i for i in range(n_in)}, compiler_params=_IN_FLIGHT,
    )(*[pltpu.with_memory_space_constraint(o, pltpu.HBM) for o in rider.operands], *after)
    return (rider, res[:n_sem], res[n_sem:n_sem + n_in], res[n_sem + n_in:-1]), res[-1]


def _split_wait(handles, after, name):
    rider, sems, thru, fresh_arrays = handles
    n_in, n_out, n_sem = len(rider.operands), len(rider.out_shapes), len(rider.scratch)
    fresh = [j for j in range(n_out) if j not in rider.aliases.values()]
    by_out = {j: i for i, j in rider.aliases.items()}
    n_data = n_in + len(fresh)

    def body(*refs):
        r_in = refs[:n_in]
        fresh_refs = refs[n_in:n_data]
        sem_refs = refs[n_data:n_data + n_sem]
        r_out = [r_in[by_out[j]] if j in by_out else fresh_refs[fresh.index(j)] for j in range(n_out)]
        rider.wait(r_in, r_out, _as_rider_sems(rider, sem_refs))

    data = list(thru) + list(fresh_arrays)
    res = pl.pallas_call(
        body, name=name, out_shape=tuple(pltpu.HBM(d.shape, d.dtype) for d in data),
        in_specs=(HBM,) * n_data + (SEM,) * n_sem + (ANY,) * len(after), out_specs=(HBM,) * n_data,
        input_output_aliases={i: i for i in range(n_data)}, compiler_params=_IN_FLIGHT,
    )(*data, *sems, *after)
    return [res[by_out[j]] if j in by_out else res[n_in + fresh.index(j)] for j in range(n_out)], res[:n_in]


def _pair_gather_rider(bufs):
    n = len(bufs)

    def copies(r_out, sems):
        x, y, c = _mesh_pos()
        out = []
        for a in range(n):
            send = functools.partial(
                    pltpu.make_async_remote_copy,
                src_ref=r_out[a].at[c], dst_ref=r_out[a].at[c], send_sem=sems[0].at[a],
                recv_sem=sems[1].at[a], device_id=(x, y, 1 - c), device_id_type=MESH)
            recv = functools.partial(
                    pltpu.make_async_remote_copy,
                src_ref=r_out[a].at[1 - c], dst_ref=r_out[a].at[1 - c], send_sem=sems[0].at[a],
                recv_sem=sems[1].at[a], device_id=(x, y, 1 - c), device_id_type=MESH)
            out.append((send, recv))
        return out

    def start(r_in, r_out, sems):
        for send, _ in copies(r_out, sems):
            send().start()

    def wait(r_in, r_out, sems):
        cps = copies(r_out, sems)
        for _, recv in cps:
            recv().wait_recv()
        for send, _ in cps:
            send().wait_send()

    return _Rider(bufs, [jax.ShapeDtypeStruct(b.shape, b.dtype) for b in bufs], {i: i for i in range(n)},
                  [pltpu.SemaphoreType.DMA((n,)), pltpu.SemaphoreType.DMA((n,))], start, wait)


def _add_own_half(g, recv, pos_arr, name):
    nb, rh, cols = g.shape[0], g.shape[-2], g.shape[-1]

    def body(pos_ref, g_ref, r_ref, send_ref, own_ref):
        s = (g_ref[...] + r_ref[...]).astype(BF16)
        send_ref[...] = s

        @pl.when(pl.program_id(0) == pos_ref[1])
        def _():
            own_ref[...] = s

    blk = pl.BlockSpec((None, rh, cols), lambda j, pos_ref: (j, 0, 0))
    g_spec = blk if g.ndim == 3 else pl.BlockSpec((None, None, rh, cols),
                                                   lambda j, pos_ref: (j, pos_ref[0], 0, 0))
    shape = jax.ShapeDtypeStruct((nb, rh, cols), BF16)
    return pl.pallas_call(
        body, name=name,
        grid_spec=pltpu.PrefetchScalarGridSpec(
            num_scalar_prefetch=1, grid=(nb,), in_specs=[g_spec, blk],
            out_specs=[blk, pl.BlockSpec((None, rh, cols), lambda j, pos_ref: (pos_ref[1], 0, 0))]),
        out_shape=[shape, shape], compiler_params=_params(),
    )(pos_arr, g, recv)


def _sum_chips(gath, pos_arr, name):
    nb, rh, cols = gath.shape

    def body(pos_ref, a_ref, b_ref, c_ref, d_ref, o_ref):
        del pos_ref
        o_ref[...] = ((a_ref[...].astype(F32) + b_ref[...].astype(F32)) + c_ref[...].astype(F32)) \
            + d_ref[...].astype(F32)

    tr = rh // 2 if (rh // 2) % 16 == 0 else rh
    specs = [pl.BlockSpec((None, tr, cols), functools.partial(lambda i, pos_ref, j: (j, i, 0), j=j))
             for j in range(nb)]
    return pl.pallas_call(
        body, name=name,
        grid_spec=pltpu.PrefetchScalarGridSpec(
            num_scalar_prefetch=1, grid=(rh // tr,), in_specs=specs,
            out_specs=pl.BlockSpec((None, tr, cols), lambda i, pos_ref: (pos_ref[0], i, 0))),
        out_shape=jax.ShapeDtypeStruct((2, rh, cols), F32), compiler_params=_params(),
    )(pos_arr, gath, gath, gath, gath)


N_DEVICES = 8


def _small_gather_rider(buf):
    def copies(r_out, sems):
        x, y, c = _mesh_pos()
        me = 4 * x + 2 * y + c
        out = []
        for r in range(1, N_DEVICES):
            px = 1 - x if r & 4 else x
            py = 1 - y if r & 2 else y
            pc = 1 - c if r & 1 else c
            out.append(pltpu.make_async_remote_copy(
                src_ref=r_out[0].at[me], dst_ref=r_out[0].at[me], send_sem=sems[0].at[r - 1],
                recv_sem=sems[1].at[r - 1], device_id=(px, py, pc), device_id_type=MESH))
        return out

    def start(r_in, r_out, sems):
        for cp in copies(r_out, sems):
            cp.start()

    def wait(r_in, r_out, sems):
        cps = copies(r_out, sems)
        for cp in cps:
            cp.wait_recv()
        for cp in cps:
            cp.wait_send()

    return _Rider([buf], [jax.ShapeDtypeStruct(buf.shape, buf.dtype)], {0: 0},
                  [pltpu.SemaphoreType.DMA((N_DEVICES - 1,)), pltpu.SemaphoreType.DMA((N_DEVICES - 1,))],
                  start, wait)


def _sum_devices(buf, name):
    def body(b_ref, o_ref):
        acc = b_ref[0]
        for i in range(1, N_DEVICES):
            acc = acc + b_ref[i]
        o_ref[...] = acc

    return _pallas(body, name=name, grid=(1,), in_specs=[pl.BlockSpec(buf.shape, lambda i: (0, 0, 0))],
                   out_specs=pl.BlockSpec(buf.shape[1:], lambda i: (0, 0)),
                   out_shape=jax.ShapeDtypeStruct(buf.shape[1:], F32), operands=[buf])


def _adamw_math(w, g, m, v):
    m = ADAM_B1 * m + (1.0 - ADAM_B1) * g
    v = ADAM_B2 * v + (1.0 - ADAM_B2) * (g * g)
    m_hat = m / (1.0 - ADAM_B1 ** ADAM_STEP)
    v_hat = v / (1.0 - ADAM_B2 ** ADAM_STEP)
    delta = -ADAM_LR * (m_hat / (jnp.sqrt(v_hat) + ADAM_EPS) + ADAM_WD * w)
    return delta, m, v


def _adamw(w, g, m, v, name, after=()):
    r, c = w.shape
    tr = 128 if r % 128 == 0 else 64
    assert r % tr == 0

    def body(w_ref, g_ref, m_ref, v_ref, go_ref, d_ref, mo_ref, vo_ref):
        gv = g_ref[...]
        d, mn, vn = _adamw_math(w_ref[...], gv, m_ref[...], v_ref[...])
        go_ref[...] = gv
        d_ref[...] = d
        mo_ref[...] = mn
        vo_ref[...] = vn

    blk = pl.BlockSpec((tr, c), lambda i: (i, 0))
    return _pallas(body, name=name, grid=(r // tr,), in_specs=[blk] * 4, out_specs=[blk] * 4,
                   out_shape=[jax.ShapeDtypeStruct((r, c), F32)] * 4, operands=[w, g, m, v], after=after)


def _adamw_small(ws, gs, ms, vs, name):
    n = len(ws)

    def body(*refs):
        w_r, g_r, m_r, v_r = refs[:n], refs[n:2 * n], refs[2 * n:3 * n], refs[3 * n:4 * n]
        d_o, m_o, v_o = refs[4 * n:5 * n], refs[5 * n:6 * n], refs[6 * n:7 * n]
        for i in range(n):
            d, mn, vn = _adamw_math(w_r[i][...], g_r[i][...], m_r[i][...], v_r[i][...])
            d_o[i][...] = d
            m_o[i][...] = mn
            v_o[i][...] = vn

    specs = [pl.BlockSpec(w.shape, lambda i: (0, 0)) for w in ws]
    shapes = [jax.ShapeDtypeStruct(w.shape, F32) for w in ws]
    outs = pl.pallas_call(
        body, name=name, grid=(1,), in_specs=specs * 4, out_specs=specs * 3, out_shape=shapes * 3,
        compiler_params=_params(),
    )(*ws, *gs, *ms, *vs)
    return outs[:n], outs[n:2 * n], outs[2 * n:]


BIG = ("w_in", "w_o_attn", "w_pw_conv", "w_out", "w_ffn_in", "w_ffn_out")
ROW_SHARDED = ("w_out", "w_ffn_out")
SMALL = ("norm1_w", "b_gate", "q_norm_w", "k_norm_w", "conv_w", "conv_b", "conv_ln_w", "conv_ln_b", "norm2_w")
ORDER = ("norm1_w", "w_in", "b_gate", "q_norm_w", "k_norm_w", "w_o_attn", "conv_w", "conv_b", "conv_ln_w",
         "conv_ln_b", "w_pw_conv", "w_out", "norm2_w", "w_ffn_in", "w_ffn_out")
PACK_TILE = 8 * LANES


def _pack_small(parts):
    rows = []
    for p in parts:
        flat = p.reshape(-1)
        pad = (-flat.shape[0]) % PACK_TILE
        rows.append(jnp.pad(flat, (0, pad)).reshape(-1, LANES))
    return jnp.concatenate(rows, axis=0)


def _unpack_small(packed, shapes):
    out, row = [], 0
    for shp in shapes:
        size = int(np.prod(shp))
        nrow = -(-size // PACK_TILE) * (PACK_TILE // LANES)
        out.append(packed[row:row + nrow].reshape(-1)[:size].reshape(shp))
        row += nrow
    return out


def kernel(x, positions, norm1_w, w_in, b_gate, q_norm_w, k_norm_w, w_o_attn, conv_w, conv_b, conv_ln_w, conv_ln_b, w_pw_conv, w_out, norm2_w, w_ffn_in, w_ffn_out, loss_target, m_norm1_w, m_w_in, m_b_gate, m_q_norm_w, m_k_norm_w, m_w_o_attn, m_conv_w, m_conv_b, m_conv_ln_w, m_conv_ln_b, m_w_pw_conv, m_w_out, m_norm2_w, m_w_ffn_in, m_w_ffn_out, v_norm1_w, v_w_in, v_b_gate, v_q_norm_w, v_k_norm_w, v_w_o_attn, v_conv_w, v_conv_b, v_conv_ln_w, v_conv_ln_b, v_w_pw_conv, v_w_out, v_norm2_w, v_w_ffn_in, v_w_ffn_out):
    w = dict(norm1_w=norm1_w, w_in=w_in, b_gate=b_gate, q_norm_w=q_norm_w, k_norm_w=k_norm_w, w_o_attn=w_o_attn,
             conv_w=conv_w, conv_b=conv_b, conv_ln_w=conv_ln_w, conv_ln_b=conv_ln_b, w_pw_conv=w_pw_conv,
             w_out=w_out, norm2_w=norm2_w, w_ffn_in=w_ffn_in, w_ffn_out=w_ffn_out)
    m = dict(norm1_w=m_norm1_w, w_in=m_w_in, b_gate=m_b_gate, q_norm_w=m_q_norm_w, k_norm_w=m_k_norm_w,
             w_o_attn=m_w_o_attn, conv_w=m_conv_w, conv_b=m_conv_b, conv_ln_w=m_conv_ln_w,
             conv_ln_b=m_conv_ln_b, w_pw_conv=m_w_pw_conv, w_out=m_w_out, norm2_w=m_norm2_w,
             w_ffn_in=m_w_ffn_in, w_ffn_out=m_w_ffn_out)
    v = dict(norm1_w=v_norm1_w, w_in=v_w_in, b_gate=v_b_gate, q_norm_w=v_q_norm_w, k_norm_w=v_k_norm_w,
             w_o_attn=v_w_o_attn, conv_w=v_conv_w, conv_b=v_conv_b, conv_ln_w=v_conv_ln_w,
             conv_ln_b=v_conv_ln_b, w_pw_conv=v_w_pw_conv, w_out=v_w_out, norm2_w=v_norm2_w,
             w_ffn_in=v_w_ffn_in, w_ffn_out=v_w_ffn_out)
    cx, cy, cc = _mesh_pos()
    chip = 2 * cx + cy

    chip_arr = chip.reshape(1).astype(jnp.int32)
    pos_arr = jnp.stack([cc, chip]).astype(jnp.int32)
    bufs = {}
    for n in BIG:
        buf = _cast_into_slot(w[n][0], chip_arr, BF16, f"cast_{n}")
        bufs[n] = buf.reshape(N_CHIPS, 2, buf.shape[1] // 2, buf.shape[2])
    small_bufs = [_cast_into_slot(w[n][0], chip_arr, F32, f"slot_{n}") for n in ("conv_w", "b_gate")]
    w_in_buf, conv_w_buf, b_gate_buf = _comm_call(_gather_both_legs_rider([bufs["w_in"]], small_bufs),
                                                  "allgather_w_in")
    wts = dict(w_in=w_in_buf.reshape(N_CHIPS, -1, w_in_buf.shape[3]),
               conv_w=conv_w_buf.transpose(1, 0, 2).reshape(CONV_WIDTH, -1),
               b_gate=b_gate_buf.transpose(1, 0, 2).reshape(2, 1, -1),
               norm1_w=norm1_w, q_norm_w=q_norm_w, k_norm_w=k_norm_w, conv_b=conv_b, conv_ln_w=conv_ln_w,
               conv_ln_b=conv_ln_b, norm2_w=norm2_w)

    loss, grad_x, g, reduced, w_in_in_flight = _forward_backward(
        x[0], positions.reshape(-1, 1), loss_target[0], wts, [bufs[n] for n in LATE_GATHER], pos_arr)
    grads = {n: b.reshape(-1, b.shape[2]) for n, b in reduced.items()}

    w_in_in_flight, started = w_in_in_flight
    delta, new_m, new_v = {}, {}, {}
    for n in EARLY_REDUCE:
        grads[n], delta[n], new_m[n], new_v[n] = _adamw(w[n][0], grads[n], m[n][0], v[n][0], f"adamw_{n}",
                                                        after=[started])
    small_parts = [loss] + [g[n] for n in SMALL]
    small_shapes = [p.shape for p in small_parts]
    device_arr = (4 * cx + 2 * cy + cc).reshape(1).astype(jnp.int32)
    small_buf = _cast_into_slot(_pack_small(small_parts), device_arr, F32, "slot_small", n_slots=N_DEVICES)

    (by_chip_w_in,), _ = _split_wait(w_in_in_flight, after=[delta[n] for n in EARLY_REDUCE] + [small_buf],
                                     name="grads_w_in_exchange_wait")
    half_w_in = _sum_chips(by_chip_w_in, pos_arr, "grads_chip_sum_w_in")
    shard_w_in, small_buf = _comm_call(
        _riders_together(_pair_gather_rider([half_w_in]), _small_gather_rider(small_buf)),
        "grads_pair_gather_w_in_small_gather")
    summed = _sum_devices(small_buf, "small_sum")
    reduced = _unpack_small(summed, small_shapes)
    loss_total = reduced[0].reshape(())
    for n, r in zip(SMALL, reduced[1:]):
        grads[n] = r
    ch_shard = conv_w.shape[2]
    grads["conv_w"] = lax.dynamic_slice_in_dim(grads["conv_w"], chip * ch_shard, ch_shard, axis=1)
    d_shard = b_gate.shape[2]
    grads["b_gate"] = lax.dynamic_slice_in_dim(grads["b_gate"], chip * d_shard, d_shard, axis=1)

    grads["w_in"], delta["w_in"], new_m["w_in"], new_v["w_in"] = _adamw(
        w["w_in"][0], shard_w_in.reshape(-1, shard_w_in.shape[2]), m["w_in"][0], v["w_in"][0], "adamw_w_in")
    flat2 = lambda a: a.reshape(-1, a.shape[-1])
    d_s, m_s, v_s = _adamw_small([flat2(w[n]) for n in SMALL], [flat2(grads[n]) for n in SMALL],
                                 [flat2(m[n]) for n in SMALL], [flat2(v[n]) for n in SMALL], "adamw_small")
    for i, n in enumerate(SMALL):
        delta[n], new_m[n], new_v[n] = d_s[i], m_s[i], v_s[i]

    shaped = lambda d, n: d[n].reshape(w[n].shape)
    return (loss_total, grad_x[None], *[shaped(grads, n) for n in ORDER], *[shaped(delta, n) for n in ORDER],
            *[shaped(new_m, n) for n in ORDER], *[shaped(new_v, n) for n in ORDER])
```

```python
import functools

import numpy as np
import jax
import jax.numpy as jnp
from jax import lax
from jax.experimental import pallas as pl
from jax.experimental.pallas import tpu as pltpu

F32 = jnp.float32
BF16 = jnp.bfloat16
MESH = pl.DeviceIdType.MESH
ANY = pl.BlockSpec(memory_space=pl.ANY)

HEAD_DIM = 64
N_SLOT_HEADS = 8
DILATIONS = (1, 4, 16)
HALF_SPAN = 64
ROPE_THETA = 500000.0
ROT_DIM = 16
CONV_WIDTH = 31
EPS = 1e-6
NEG_INF = -1e30
ADAM_LR, ADAM_B1, ADAM_B2, ADAM_EPS, ADAM_WD, ADAM_STEP = 0.001, 0.9, 0.999, 1e-08, 0.01, 10

LANES = 128
QBLK = 128
KWIN = QBLK + 2 * HALF_SPAN
VMEM_LIMIT = 48 * 1024 * 1024
N_CHIPS = 4


def _params(**kw):
    return pltpu.CompilerParams(vmem_limit_bytes=VMEM_LIMIT, **kw)


class _Rider:
    def __init__(self, operands, out_shapes, aliases, scratch, start, wait, middle=None):
        self.operands, self.out_shapes, self.aliases = list(operands), list(out_shapes), dict(aliases)
        self.scratch, self.start, self.wait = list(scratch), start, wait
        self.middle = middle


def _riders_together(a, b):
    n_in, n_out, n_sc = len(a.operands), len(a.out_shapes), len(a.scratch)
    aliases = dict(a.aliases)
    aliases.update({n_in + src: n_out + dst for src, dst in b.aliases.items()})

    def start(r_in, r_out, r_sc):
        a.start(r_in[:n_in], r_out[:n_out], r_sc[:n_sc])
        b.start(r_in[n_in:], r_out[n_out:], r_sc[n_sc:])

    def wait(r_in, r_out, r_sc):
        a.wait(r_in[:n_in], r_out[:n_out], r_sc[:n_sc])
        b.wait(r_in[n_in:], r_out[n_out:], r_sc[n_sc:])

    return _Rider(a.operands + b.operands, a.out_shapes + b.out_shapes, aliases, a.scratch + b.scratch, start, wait)


def _pallas(body, *, name, grid, in_specs, out_specs, out_shape, operands, scratch_shapes=(), aliases=None,
            rider=None, after=()):
    single = not isinstance(out_specs, (list, tuple))
    out_specs_l = [out_specs] if single else list(out_specs)
    out_shape_l = [out_shape] if single else list(out_shape)
    aliases = dict(aliases or {})

    def call(fn, all_in_specs, all_out_specs, all_out_shape, all_scratch, all_aliases, all_operands):
        return pl.pallas_call(
            fn, name=name, grid=grid, in_specs=all_in_specs, out_specs=all_out_specs, out_shape=all_out_shape,
            scratch_shapes=all_scratch, input_output_aliases=all_aliases, compiler_params=_params(),
        )(*all_operands)

    if rider is None:
        n_main = len(in_specs)

        def ordered(*refs):
            body(*refs[:n_main], *refs[n_main + len(after):])

        res = call(ordered if after else body, list(in_specs) + [ANY] * len(after), out_specs_l, out_shape_l,
                   list(scratch_shapes), aliases, list(operands) + list(after))
        return res[0] if single else res
    assert not after
    n_in, n_rin = len(in_specs), len(rider.operands)
    n_out, n_rout = len(out_specs_l), len(rider.out_shapes)
    n_sc = len(scratch_shapes)

    def wrapped(*refs):
        main_in, r_in = refs[:n_in], refs[n_in:n_in + n_rin]
        o0 = n_in + n_rin
        main_out, r_out = refs[o0:o0 + n_out], refs[o0 + n_out:o0 + n_out + n_rout]
        s0 = o0 + n_out + n_rout
        main_sc, r_sc = refs[s0:s0 + n_sc], refs[s0 + n_sc:]
        ids = [pl.program_id(d) for d in range(len(grid))]
        first = functools.reduce(jnp.logical_and, [i == 0 for i in ids])
        last = functools.reduce(jnp.logical_and, [i == n - 1 for i, n in zip(ids, grid)])

        @pl.when(first)
        def _():
            rider.start(r_in, r_out, r_sc)

        body(*main_in, *main_out, *main_sc)

        @pl.when(last)
        def _():
            rider.wait(r_in, r_out, r_sc)

    for src, dst in rider.aliases.items():
        aliases[n_in + src] = n_out + dst
    res = call(wrapped, list(in_specs) + [ANY] * n_rin, out_specs_l + [ANY] * n_rout,
               out_shape_l + rider.out_shapes, list(scratch_shapes) + rider.scratch, aliases,
               list(operands) + rider.operands)
    main = res[:n_out]
    return (main[0] if single else main), res[n_out:]


def _matmul(a, b, *, mode, tm, tn, tk, out_dtype, name, b_blocked=False,
            out_blocked=None, cols_outer=False, rider=None, after=()):
    a_shape = a.shape
    if mode == "nn":
        m_dim, k_dim = a_shape
        n_dim = b.shape[0] * b.shape[2] if b_blocked else b.shape[1]
        rows, cols, red = m_dim, n_dim, k_dim
    elif mode == "nt":
        m_dim, n_dim = a_shape
        k_dim = b.shape[1] if b_blocked else b.shape[0]
        rows, cols, red = m_dim, k_dim, n_dim
    else:
        m_dim, k_dim = a_shape
        n_dim = b.shape[1]
        rows, cols, red = k_dim, n_dim, m_dim
    assert rows % tm == 0 and cols % tn == 0 and red % tk == 0, (name, rows, cols, red)
    ni, nj, nk = rows // tm, cols // tn, red // tk

    if mode == "nn":
        a_spec = pl.BlockSpec((tm, tk), lambda i, j, k: (i, k))
        if b_blocked:
            per = b.shape[2] // tn
            b_spec = pl.BlockSpec((None, tk, tn), lambda i, j, k: (j // per, k, j % per))
        else:
            b_spec = pl.BlockSpec((tk, tn), lambda i, j, k: (k, j))
        dims = (((1,), (0,)), ((), ()))
    elif mode == "nt":
        a_spec = pl.BlockSpec((tm, tk), lambda i, j, k: (i, k))
        if b_blocked:
            per = b.shape[2] // tk
            b_spec = pl.BlockSpec((None, tn, tk), lambda i, j, k: (k // per, j, k % per))
        else:
            b_spec = pl.BlockSpec((tn, tk), lambda i, j, k: (j, k))
        dims = (((1,), (1,)), ((), ()))
    else:
        a_spec = pl.BlockSpec((tk, tm), lambda i, j, k: (k, i))
        b_spec = pl.BlockSpec((tk, tn), lambda i, j, k: (k, j))
        dims = (((0,), (0,)), ((), ()))

    if out_blocked:
        per_o = (cols // out_blocked) // tn
        out_spec = pl.BlockSpec((None, tm, tn), lambda i, j, k: (j // per_o, i, j % per_o))
        out_shape = jax.ShapeDtypeStruct((out_blocked, rows, cols // out_blocked), out_dtype)
    else:
        out_spec = pl.BlockSpec((tm, tn), lambda i, j, k: (i, j))
        out_shape = jax.ShapeDtypeStruct((rows, cols), out_dtype)

    def body(a_ref, b_ref, o_ref, *acc):
        prod = lax.dot_general(a_ref[...], b_ref[...], dims, preferred_element_type=F32)
        if nk == 1:
            o_ref[...] = prod.astype(out_dtype)
        else:
            acc_ref, = acc
            k = pl.program_id(2)

            @pl.when(k == 0)
            def _():
                acc_ref[...] = prod

            @pl.when(k > 0)
            def _():
                acc_ref[...] += prod

            @pl.when(k == nk - 1)
            def _():
                o_ref[...] = acc_ref[...].astype(out_dtype)

    scratch = [pltpu.VMEM((tm, tn), F32)] if nk > 1 else []
    grid = (ni, nj, nk)
    if cols_outer:
        swap = lambda spec: pl.BlockSpec(spec.block_shape, lambda j, i, k, f=spec.index_map: f(i, j, k))
        a_spec, b_spec, out_spec, grid = swap(a_spec), swap(b_spec), swap(out_spec), (nj, ni, nk)
    return _pallas(body, name=name, grid=grid, in_specs=[a_spec, b_spec], out_specs=out_spec,
                   out_shape=out_shape, operands=[a, b], scratch_shapes=scratch, rider=rider, after=after)


def _rmsnorm_fwd(x, w, name):
    s, d = x.shape
    tm = 256

    def body(x_ref, w_ref, o_ref):
        xv = x_ref[...]
        rstd = lax.rsqrt(jnp.mean(xv * xv, axis=-1, keepdims=True) + EPS)
        o_ref[...] = (xv * rstd * w_ref[...]).astype(BF16)

    return pl.pallas_call(
        body, name=name, grid=(s // tm,),
        in_specs=[pl.BlockSpec((tm, d), lambda i: (i, 0)), pl.BlockSpec((1, d), lambda i: (0, 0))],
        out_specs=pl.BlockSpec((tm, d), lambda i: (i, 0)),
        out_shape=jax.ShapeDtypeStruct((s, d), BF16), compiler_params=_params(),
    )(x, w)


def _rmsnorm_bwd(dh, x, w, dres, name, rider=None):
    s, d = x.shape
    tm = 256

    def body(dh_ref, x_ref, w_ref, dres_ref, dx_ref, dxb_ref, dw_ref):
        xv = x_ref[...]
        rstd = lax.rsqrt(jnp.mean(xv * xv, axis=-1, keepdims=True) + EPS)
        xhat = xv * rstd
        dhv = dh_ref[...]
        g = dhv * w_ref[...]
        dx = rstd * (g - xhat * jnp.mean(g * xhat, axis=-1, keepdims=True)) + dres_ref[...]
        dx_ref[...] = dx
        dxb_ref[...] = dx.astype(BF16)
        part = jnp.sum(dhv * xhat, axis=0, keepdims=True)

        @pl.when(pl.program_id(0) == 0)
        def _():
            dw_ref[...] = part

        @pl.when(pl.program_id(0) > 0)
        def _():
            dw_ref[...] += part

    row = pl.BlockSpec((tm, d), lambda i: (i, 0))
    vec = pl.BlockSpec((1, d), lambda i: (0, 0))
    return _pallas(
        body, name=name, grid=(s // tm,), in_specs=[row, row, vec, row], out_specs=[row, row, vec],
        out_shape=[jax.ShapeDtypeStruct((s, d), F32), jax.ShapeDtypeStruct((s, d), BF16),
                   jax.ShapeDtypeStruct((1, d), F32)],
        operands=[dh, x, w, dres], rider=rider)


def _rope_consts():
    lane = np.arange(LANES)
    in_head = lane % HEAD_DIM
    inv_freq = ROPE_THETA ** (-jnp.arange(0, ROT_DIM, 2, dtype=F32) / ROT_DIM)
    invf = jnp.where(jnp.asarray(in_head < ROT_DIM), jnp.tile(inv_freq, LANES // (ROT_DIM // 2)), 0.0)
    m_a = np.where(in_head < ROT_DIM // 2, -1.0, 0.0).astype(np.float32)
    m_b = np.where((in_head >= ROT_DIM // 2) & (in_head < ROT_DIM), 1.0, 0.0).astype(np.float32)
    block_diag = (lane[:, None] // HEAD_DIM == lane[None, :] // HEAD_DIM).astype(np.float32)
    return (invf.reshape(1, LANES).astype(F32), jnp.asarray(m_a).reshape(1, LANES),
            jnp.asarray(m_b).reshape(1, LANES), jnp.asarray(block_diag, dtype=BF16))


def _head_sums(v, bd):
    hi = v.astype(BF16)
    lo = (v - hi.astype(F32)).astype(BF16)
    return jnp.dot(hi, bd, preferred_element_type=F32) + jnp.dot(lo, bd, preferred_element_type=F32)


def _qk_fwd(proj, pos_col, qw2, kw2, consts, name, rider=None):
    s = proj.shape[0]
    width = 3 * N_SLOT_HEADS * HEAD_DIM
    tm = 128
    invf, m_a, m_b, bd = consts
    scale = HEAD_DIM ** -0.5

    def body(q_ref, k_ref, pos_ref, qw_ref, kw_ref, invf_ref, ma_ref, mb_ref, bd_ref, qo_ref, ko_ref):
        ang = pos_ref[...].astype(F32) * invf_ref[...]
        cos = jnp.cos(ang)
        sin = jnp.sin(ang)
        s_a = sin * ma_ref[...]
        s_b = sin * mb_ref[...]
        bdv = bd_ref[...]
        for src, w_ref, dst, sc in ((q_ref, qw_ref, qo_ref, scale), (k_ref, kw_ref, ko_ref, 1.0)):
            for cb in range(width // LANES):
                cols = slice(cb * LANES, (cb + 1) * LANES)
                t = src[:, cols]
                rstd = lax.rsqrt(_head_sums(t * t, bdv) * (1.0 / HEAD_DIM) + EPS)
                y = t * rstd * w_ref[...]
                r = y * cos + pltpu.roll(y, LANES - 8, axis=1) * s_a + pltpu.roll(y, 8, axis=1) * s_b
                dst[:, cols] = r * sc if sc != 1.0 else r

    vec = pl.BlockSpec((1, LANES), lambda i: (0, 0))
    return _pallas(
        body, name=name, grid=(s // tm,),
        in_specs=[pl.BlockSpec((tm, width), lambda i: (i, 0)), pl.BlockSpec((tm, width), lambda i: (i, 1)),
                  pl.BlockSpec((tm, 1), lambda i: (i, 0)), vec, vec, vec, vec, vec,
                  pl.BlockSpec((LANES, LANES), lambda i: (0, 0))],
        out_specs=[pl.BlockSpec((tm, width), lambda i: (i, 0))] * 2,
        out_shape=[jax.ShapeDtypeStruct((s, width), F32)] * 2,
        operands=[proj, proj, pos_col, qw2, kw2, invf, m_a, m_b, bd], rider=rider)


def _qk_bwd(dqn, dkn, dv, da, db, dgl, proj, pos_col, qw2, kw2, consts, name, rider=None):
    s = proj.shape[0]
    width = 3 * N_SLOT_HEADS * HEAD_DIM
    ch = da.shape[1]
    gate_w = dgl.shape[1]
    out_w = 3 * width + 2 * ch + gate_w
    assert out_w == proj.shape[1]
    tm = 128
    invf, m_a, m_b, bd = consts
    scale = HEAD_DIM ** -0.5

    def body(dq_ref, dk_ref, dv_ref, da_ref, db_ref, dgl_ref, q_ref, k_ref, pos_ref, qw_ref, kw_ref,
             invf_ref, ma_ref, mb_ref, bd_ref, out_ref, dqw_ref, dkw_ref):
        ang = pos_ref[...].astype(F32) * invf_ref[...]
        cos = jnp.cos(ang)
        sin = jnp.sin(ang)
        s_a = sin * ma_ref[...]
        s_b = sin * mb_ref[...]
        bdv = bd_ref[...]
        first = pl.program_id(0) == 0
        for src, dsrc, w_ref, col0, dw_ref, sc in ((q_ref, dq_ref, qw_ref, 0, dqw_ref, scale),
                                                   (k_ref, dk_ref, kw_ref, width, dkw_ref, 1.0)):
            dw_acc = jnp.zeros((1, LANES), F32)
            for cb in range(width // LANES):
                cols = slice(cb * LANES, (cb + 1) * LANES)
                t = src[:, cols]
                dr = dsrc[:, cols]
                if sc != 1.0:
                    dr = dr * sc
                dy = dr * cos + pltpu.roll(dr * s_a, 8, axis=1) + pltpu.roll(dr * s_b, LANES - 8, axis=1)
                rstd = lax.rsqrt(_head_sums(t * t, bdv) * (1.0 / HEAD_DIM) + EPS)
                xhat = t * rstd
                g = dy * w_ref[...]
                dt = rstd * (g - xhat * (_head_sums(g * xhat, bdv) * (1.0 / HEAD_DIM)))
                out_ref[:, col0 + cb * LANES: col0 + (cb + 1) * LANES] = dt.astype(BF16)
                dw_acc = dw_acc + jnp.sum(dy * xhat, axis=0, keepdims=True)
            dw_acc = dw_acc + pltpu.roll(dw_acc, HEAD_DIM, axis=1)

            @pl.when(first)
            def _(dw_ref=dw_ref, dw_acc=dw_acc):
                dw_ref[...] = dw_acc

            @pl.when(jnp.logical_not(first))
            def _(dw_ref=dw_ref, dw_acc=dw_acc):
                dw_ref[...] += dw_acc
        out_ref[:, 2 * width: 3 * width] = dv_ref[...].astype(BF16)
        out_ref[:, 3 * width: 3 * width + ch] = da_ref[...]
        out_ref[:, 3 * width + ch: 3 * width + 2 * ch] = db_ref[...]
        out_ref[:, 3 * width + 2 * ch: out_w] = dgl_ref[...]

    vec = pl.BlockSpec((1, LANES), lambda i: (0, 0))
    blk = lambda c: pl.BlockSpec((tm, width), lambda i: (i, c))
    cblk = pl.BlockSpec((tm, ch), lambda i: (i, 0))
    return _pallas(
        body, name=name, grid=(s // tm,),
        in_specs=[blk(0), blk(0), blk(0), cblk, cblk, pl.BlockSpec((tm, gate_w), lambda i: (i, 0)),
                  blk(0), blk(1), pl.BlockSpec((tm, 1), lambda i: (i, 0)), vec, vec, vec, vec, vec,
                  pl.BlockSpec((LANES, LANES), lambda i: (0, 0))],
        out_specs=[pl.BlockSpec((tm, out_w), lambda i: (i, 0)), vec, vec],
        out_shape=[jax.ShapeDtypeStruct((s, out_w), BF16)] + [jax.ShapeDtypeStruct((1, LANES), F32)] * 2,
        operands=[dqn, dkn, dv, da, db, dgl, proj, proj, pos_col, qw2, kw2, invf, m_a, m_b, bd],
        rider=rider)


def _row_chunks(n_rows, fn, chunk=256):
    def step(i, c):
        fn(pl.ds(pl.multiple_of(i * chunk, chunk), chunk))
        return c
    lax.fori_loop(0, n_rows // chunk, step, 0)


def _to_residue_major(dst, src, s, d, dst_off=0, cast=None):
    seq = s // d
    for r in range(d):
        v = src[...] if d == 1 else src[pl.ds(r, seq, stride=d), :]
        dst[dst_off + r * seq: dst_off + (r + 1) * seq, :] = v if cast is None else v.astype(cast)


def _from_residue_major(dst, src, s, d, src_off=0):
    seq = s // d
    for r in range(d):
        v = src[src_off + r * seq: src_off + (r + 1) * seq, :]
        if d == 1:
            dst[...] = v
        else:
            dst[pl.ds(r, seq, stride=d), :] = v


def _band_bias():
    qi = lax.broadcasted_iota(jnp.int32, (QBLK, KWIN), 0)
    kj = lax.broadcasted_iota(jnp.int32, (QBLK, KWIN), 1)
    return jnp.where(jnp.abs(kj - HALF_SPAN - qi) <= HALF_SPAN, 0.0, NEG_INF).astype(F32)


def _range_bias(base, seq):
    kj = lax.broadcasted_iota(jnp.int32, (1, KWIN), 1)
    lo = (base & -seq) - base + HALF_SPAN
    return jnp.where((kj >= lo) & (kj < lo + seq), 0.0, NEG_INF).astype(F32)


def _skewed_blocks(n_blk, produce, consume):
    produce(0, 0)
    for b in range(n_blk):
        consume(b, b % 2)
        if b + 1 < n_blk:
            produce(b + 1, (b + 1) % 2)


def _block_base(b):
    return b * QBLK if isinstance(b, int) else pl.multiple_of(b * QBLK, QBLK)


def _attn_fwd(qn, kn, proj, name, rider=None):
    s = qn.shape[0]
    n_pairs = N_SLOT_HEADS * HEAD_DIM // LANES
    v_col0 = 2 * qn.shape[1] // LANES
    nt_dims = (((1,), (1,)), ((), ()))

    def body(q_ref, k_ref, v_ref, attn_ref, lse_ref, attn_b_ref, q_rm, k_rm, v_rm, acc_rm, m_rm, l_rm,
             acc_p, m_p, l_p, m_run, l_run, acc_run, band, s_buf, m_buf):
        g = pl.program_id(1)
        zpad = jnp.zeros((HALF_SPAN, LANES), BF16)
        k_rm[0:HALF_SPAN, :] = zpad
        k_rm[s + HALF_SPAN: s + 2 * HALF_SPAN, :] = zpad
        v_rm[0:HALF_SPAN, 0:LANES] = zpad
        v_rm[s + HALF_SPAN: s + 2 * HALF_SPAN, 0:LANES] = zpad

        def ones_rows(rows):
            v_rm[pl.ds(rows.start, rows.size), LANES:2 * LANES] = jnp.ones((rows.size, LANES), BF16)

        _row_chunks(s + 2 * HALF_SPAN, ones_rows, chunk=2 * HALF_SPAN)
        band[...] = _band_bias()
        lane = lax.broadcasted_iota(jnp.int32, (QBLK, LANES), 1)
        low = lane < HEAD_DIM
        n_blk = s // QBLK

        for gi, d in enumerate(DILATIONS):
            @pl.when(g == gi)
            def _(gi=gi, d=d):
                seq = s // d
                _to_residue_major(q_rm, q_ref, s, d, cast=BF16)
                _to_residue_major(k_rm, k_ref, s, d, dst_off=HALF_SPAN, cast=BF16)
                _to_residue_major(v_rm.at[:, 0:LANES], v_ref, s, d, dst_off=HALF_SPAN, cast=BF16)

                def scores(b, slot):
                    base = _block_base(b)
                    q = q_rm[pl.ds(base, QBLK), :]
                    zero = jnp.zeros_like(q)
                    q2 = jnp.concatenate([jnp.where(low, q, zero), jnp.where(low, zero, q)], axis=0)
                    sc = lax.dot_general(q2, k_rm[pl.ds(base, KWIN), :], nt_dims, preferred_element_type=F32)
                    bias = band[...] + _range_bias(base, seq)
                    for hh in range(2):
                        rows = slice(hh * QBLK, (hh + 1) * QBLK)
                        sh = sc[rows, :] + bias
                        s_buf[slot, rows, :] = sh
                        m_buf[slot, rows, :] = jnp.broadcast_to(jnp.max(sh, axis=-1, keepdims=True), (QBLK, LANES))

                def outputs(b, slot):
                    base = _block_base(b)
                    sv = s_buf[slot]
                    mb = m_buf[slot]
                    p = jnp.exp(jnp.concatenate([sv[:, 0:LANES] - mb, sv[:, LANES:2 * LANES] - mb], axis=1))
                    pv = jnp.dot(p.astype(BF16), v_rm[pl.ds(base, KWIN), :], preferred_element_type=F32)
                    rows = pl.ds(base, QBLK)
                    acc_rm[rows, :] = jnp.where(low, pv[0:QBLK, 0:LANES], pv[QBLK:2 * QBLK, 0:LANES])
                    l_rm[rows, :] = jnp.where(low, pv[0:QBLK, LANES:2 * LANES], pv[QBLK:2 * QBLK, LANES:2 * LANES])
                    m_rm[rows, :] = jnp.where(low, mb[0:QBLK, :], mb[QBLK:2 * QBLK, :])

                _skewed_blocks(n_blk, scores, outputs)
                if d == 1:
                    src = (acc_rm, m_rm, l_rm)
                else:
                    for dst_, src_ in ((acc_p, acc_rm), (m_p, m_rm), (l_p, l_rm)):
                        _from_residue_major(dst_, src_, s, d)
                    src = (acc_p, m_p, l_p)

                def combine(rows):
                    a_g, m_g, l_g = src[0][rows, :], src[1][rows, :], src[2][rows, :]
                    if gi == 0:
                        m_new, l_new, a_new = m_g, l_g, a_g
                    else:
                        m_old = m_run[rows, :]
                        m_new = jnp.maximum(m_old, m_g)
                        w_old = jnp.exp(m_old - m_new)
                        w_g = jnp.exp(m_g - m_new)
                        l_new = l_run[rows, :] * w_old + l_g * w_g
                        a_new = acc_run[rows, :] * w_old + a_g * w_g
                    if gi == len(DILATIONS) - 1:
                        out = a_new / l_new
                        attn_ref[rows, :] = out
                        attn_b_ref[rows, :] = out.astype(BF16)
                        lse_ref[rows, :] = m_new + jnp.log(l_new)
                    else:
                        m_run[rows, :] = m_new
                        l_run[rows, :] = l_new
                        acc_run[rows, :] = a_new

                _row_chunks(s, combine)

    qk_spec = pl.BlockSpec((s, LANES), lambda hp, g: (0, g * n_pairs + hp))
    v_spec = pl.BlockSpec((s, LANES), lambda hp, g: (0, v_col0 + g * n_pairs + hp))
    o_spec = pl.BlockSpec((s, LANES), lambda hp, g: (0, hp))
    f32buf = pltpu.VMEM((s, LANES), F32)
    return _pallas(
        body, name=name, grid=(n_pairs, len(DILATIONS)), in_specs=[qk_spec, qk_spec, v_spec],
        out_specs=[o_spec, o_spec, o_spec],
        out_shape=[jax.ShapeDtypeStruct((s, n_pairs * LANES), F32)] * 2
        + [jax.ShapeDtypeStruct((s, n_pairs * LANES), BF16)],
        operands=[qn, kn, proj],
        scratch_shapes=[pltpu.VMEM((s, LANES), BF16), pltpu.VMEM((s + 2 * HALF_SPAN, LANES), BF16),
                        pltpu.VMEM((s + 2 * HALF_SPAN, 2 * LANES), BF16)] + [f32buf] * 9
        + [pltpu.VMEM((QBLK, KWIN), F32), pltpu.VMEM((2, 2 * QBLK, KWIN), F32),
           pltpu.VMEM((2, 2 * QBLK, LANES), F32)],
        rider=rider)


def _attn_bwd(qn, kn, proj, dattn, attn, lse, bd, name, rider=None):
    s = qn.shape[0]
    n_pairs = N_SLOT_HEADS * HEAD_DIM // LANES
    v_col0 = 2 * qn.shape[1] // LANES
    nt_dims = (((1,), (1,)), ((), ()))
    tn_dims = (((0,), (0,)), ((), ()))
    spad = s + 2 * HALF_SPAN

    def body(q_ref, k_ref, v_ref, do_ref, o_ref, lse_ref, bd_ref, dq_ref, dk_ref, dv_ref,
             q_rm, k_rm, v_rm, do_rm, lse0_rm, lse1_rm, dd0_rm, dd1_rm, dq_rm, dk_rm, dv_rm,
             lse0_p, lse1_p, dd0_p, dd1_p, band, p_buf, ds_buf):
        g = pl.program_id(1)
        zpad = jnp.zeros((HALF_SPAN, LANES), BF16)
        for buf in (k_rm, v_rm):
            buf[0:HALF_SPAN, :] = zpad
            buf[s + HALF_SPAN: spad, :] = zpad
        zf = jnp.zeros((HALF_SPAN, LANES), F32)
        for buf in (dk_rm, dv_rm):
            buf[0:HALF_SPAN, :] = zf
            buf[s + HALF_SPAN: spad, :] = zf
        band[...] = _band_bias()

        def clear(rows):
            z = jnp.zeros((rows.size, LANES), F32)
            dk_rm[pl.ds(rows.start + HALF_SPAN, rows.size), :] = z
            dv_rm[pl.ds(rows.start + HALF_SPAN, rows.size), :] = z

        _row_chunks(s, clear)

        def prepare(rows):
            lo = lax.broadcasted_iota(jnp.int32, (rows.size, LANES), 1) < HEAD_DIM
            dsum = _head_sums(do_ref[rows, :] * o_ref[rows, :], bd_ref[...])
            dswap = pltpu.roll(dsum, HEAD_DIM, axis=1)
            dd0_p[rows, :] = jnp.where(lo, dsum, dswap)
            dd1_p[rows, :] = jnp.where(lo, dswap, dsum)
            lv = lse_ref[rows, :]
            lswap = pltpu.roll(lv, HEAD_DIM, axis=1)
            lse0_p[rows, :] = jnp.where(lo, lv, lswap)
            lse1_p[rows, :] = jnp.where(lo, lswap, lv)

        @pl.when(g == 0)
        def _():
            _row_chunks(s, prepare)
        lane = lax.broadcasted_iota(jnp.int32, (QBLK, LANES), 1)
        low = lane < HEAD_DIM
        n_blk = s // QBLK

        def stacked(ref, rows):
            val = ref[rows, :]
            zero = jnp.zeros_like(val)
            return jnp.concatenate([jnp.where(low, val, zero), jnp.where(low, zero, val)], axis=0)

        for gi, d in enumerate(DILATIONS):
            @pl.when(g == gi)
            def _(d=d):
                seq = s // d
                _to_residue_major(q_rm, q_ref, s, d, cast=BF16)
                _to_residue_major(k_rm, k_ref, s, d, dst_off=HALF_SPAN, cast=BF16)
                _to_residue_major(v_rm, v_ref, s, d, dst_off=HALF_SPAN, cast=BF16)
                _to_residue_major(do_rm, do_ref, s, d, cast=BF16)
                for dst_, src_ in ((lse0_rm, lse0_p), (lse1_rm, lse1_p), (dd0_rm, dd0_p), (dd1_rm, dd1_p)):
                    _to_residue_major(dst_, src_, s, d)

                def scores(b, slot):
                    base = _block_base(b)
                    rows = pl.ds(base, QBLK)
                    win = pl.ds(base, KWIN)
                    sc = lax.dot_general(stacked(q_rm, rows), k_rm[win, :], nt_dims, preferred_element_type=F32)
                    dp = lax.dot_general(stacked(do_rm, rows), v_rm[win, :], nt_dims, preferred_element_type=F32)
                    bias = band[...] + _range_bias(base, seq)
                    for hh, (lse_r, dd_r) in enumerate(((lse0_rm, dd0_rm), (lse1_rm, dd1_rm))):
                        r = slice(hh * QBLK, (hh + 1) * QBLK)
                        lse_h = lse_r[rows, :]
                        dd_h = dd_r[rows, :]
                        sh = sc[r, :] + bias
                        p = jnp.exp(jnp.concatenate([sh[:, 0:LANES] - lse_h, sh[:, LANES:KWIN] - lse_h], axis=1))
                        dph = dp[r, :]
                        ds = p * jnp.concatenate([dph[:, 0:LANES] - dd_h, dph[:, LANES:KWIN] - dd_h], axis=1)
                        p_buf[slot, r, :] = p.astype(BF16)
                        ds_buf[slot, r, :] = ds.astype(BF16)

                def grads(b, slot):
                    base = _block_base(b)
                    rows = pl.ds(base, QBLK)
                    win = pl.ds(base, KWIN)
                    p = p_buf[slot]
                    ds = ds_buf[slot]
                    dq2 = jnp.dot(ds, k_rm[win, :], preferred_element_type=F32)
                    dq_rm[rows, :] = jnp.where(low, dq2[0:QBLK, :], dq2[QBLK:2 * QBLK, :])
                    dk_rm[win, :] += lax.dot_general(ds, stacked(q_rm, rows), tn_dims, preferred_element_type=F32)
                    dv_rm[win, :] += lax.dot_general(p, stacked(do_rm, rows), tn_dims, preferred_element_type=F32)

                _skewed_blocks(n_blk, scores, grads)
                _from_residue_major(dq_ref, dq_rm, s, d)
                _from_residue_major(dk_ref, dk_rm, s, d, src_off=HALF_SPAN)
                _from_residue_major(dv_ref, dv_rm, s, d, src_off=HALF_SPAN)

    qk_spec = pl.BlockSpec((s, LANES), lambda hp, g: (0, g * n_pairs + hp))
    v_spec = pl.BlockSpec((s, LANES), lambda hp, g: (0, v_col0 + g * n_pairs + hp))
    o_spec = pl.BlockSpec((s, LANES), lambda hp, g: (0, hp))
    width = qn.shape[1]
    f32buf = pltpu.VMEM((s, LANES), F32)
    f32pad = pltpu.VMEM((spad, LANES), F32)
    return _pallas(
        body, name=name, grid=(n_pairs, len(DILATIONS)),
        in_specs=[qk_spec, qk_spec, v_spec, o_spec, o_spec, o_spec,
                  pl.BlockSpec((LANES, LANES), lambda hp, g: (0, 0))],
        out_specs=[qk_spec, qk_spec, qk_spec],
        out_shape=[jax.ShapeDtypeStruct((s, width), F32)] * 3,
        operands=[qn, kn, proj, dattn, attn, lse, bd],
        scratch_shapes=[pltpu.VMEM((s, LANES), BF16), pltpu.VMEM((spad, LANES), BF16),
                        pltpu.VMEM((spad, LANES), BF16), pltpu.VMEM((s, LANES), BF16),
                        f32buf, f32buf, f32buf, f32buf, f32buf, f32pad, f32pad,
                        f32buf, f32buf, f32buf, f32buf, pltpu.VMEM((QBLK, KWIN), F32),
                        pltpu.VMEM((2, 2 * QBLK, KWIN), BF16), pltpu.VMEM((2, 2 * QBLK, KWIN), BF16)],
        rider=rider)


CONV_PAD = 16


def _conv_fwd(proj, conv_w, conv_b, col0, name, rider=None):
    s = proj.shape[0]
    ch = conv_w.shape[1]
    nblk = ch // LANES
    a0 = col0 // LANES
    tr = 256
    shift = CONV_PAD - (CONV_WIDTH - 1) // 2

    def body(a_ref, b_ref, w_ref, bias_ref, u0_ref, uc_ref, pad):
        z = jnp.zeros((CONV_PAD, LANES), F32)
        pad[0:CONV_PAD, :] = z
        pad[s + CONV_PAD: s + 2 * CONV_PAD, :] = z

        def glu(rows):
            u0 = a_ref[rows, :] * jax.nn.sigmoid(b_ref[rows, :])
            u0_ref[rows, :] = u0
            pad[pl.ds(rows.start + CONV_PAD, rows.size), :] = u0

        _row_chunks(s, glu)
        for t in range(0, s, tr):
            acc = jnp.broadcast_to(bias_ref[...], (tr, LANES))
            for k in range(CONV_WIDTH):
                acc = acc + w_ref[k:k + 1, :] * pad[t + k + shift: t + k + shift + tr, :]
            uc_ref[t:t + tr, :] = acc

    return _pallas(
        body, name=name, grid=(nblk,),
        in_specs=[pl.BlockSpec((s, LANES), lambda c: (0, a0 + c)),
                  pl.BlockSpec((s, LANES), lambda c: (0, a0 + nblk + c)),
                  pl.BlockSpec((CONV_WIDTH, LANES), lambda c: (0, c)),
                  pl.BlockSpec((1, LANES), lambda c: (0, c))],
        out_specs=[pl.BlockSpec((s, LANES), lambda c: (0, c))] * 2,
        out_shape=[jax.ShapeDtypeStruct((s, ch), F32)] * 2, operands=[proj, proj, conv_w, conv_b],
        scratch_shapes=[pltpu.VMEM((s + 2 * CONV_PAD, LANES), F32)], rider=rider)


def _ln_silu_fwd(uc, ln_w, ln_b, name):
    s, ch = uc.shape
    tm = 256

    def body(u_ref, w_ref, b_ref, o_ref):
        u = u_ref[...]
        mu = jnp.mean(u, axis=-1, keepdims=True)
        xc = u - mu
        rstd = lax.rsqrt(jnp.mean(xc * xc, axis=-1, keepdims=True) + EPS)
        z = xc * rstd * w_ref[...] + b_ref[...]
        o_ref[...] = (z * jax.nn.sigmoid(z)).astype(BF16)

    row = pl.BlockSpec((tm, ch), lambda i: (i, 0))
    vec = pl.BlockSpec((1, ch), lambda i: (0, 0))
    return pl.pallas_call(
        body, name=name, grid=(s // tm,), in_specs=[row, vec, vec], out_specs=row,
        out_shape=jax.ShapeDtypeStruct((s, ch), BF16), compiler_params=_params(),
    )(uc, ln_w, ln_b)


def _ln_silu_bwd(du3, uc, ln_w, ln_b, name):
    s, ch = uc.shape
    tm = 256

    def body(d_ref, u_ref, w_ref, b_ref, du_ref, dw_ref, db_ref):
        u = u_ref[...]
        mu = jnp.mean(u, axis=-1, keepdims=True)
        xc = u - mu
        rstd = lax.rsqrt(jnp.mean(xc * xc, axis=-1, keepdims=True) + EPS)
        xhat = xc * rstd
        z = xhat * w_ref[...] + b_ref[...]
        sg = jax.nn.sigmoid(z)
        dz = d_ref[...] * (sg * (1.0 + z * (1.0 - sg)))
        dxh = dz * w_ref[...]
        du_ref[...] = rstd * (dxh - jnp.mean(dxh, axis=-1, keepdims=True)
                              - xhat * jnp.mean(dxh * xhat, axis=-1, keepdims=True))
        pw = jnp.sum(dz * xhat, axis=0, keepdims=True)
        pb = jnp.sum(dz, axis=0, keepdims=True)
        first = pl.program_id(0) == 0

        @pl.when(first)
        def _():
            dw_ref[...] = pw
            db_ref[...] = pb

        @pl.when(jnp.logical_not(first))
        def _():
            dw_ref[...] += pw
            db_ref[...] += pb

    row = pl.BlockSpec((tm, ch), lambda i: (i, 0))
    vec = pl.BlockSpec((1, ch), lambda i: (0, 0))
    return pl.pallas_call(
        body, name=name, grid=(s // tm,), in_specs=[row, row, vec, vec], out_specs=[row, vec, vec],
        out_shape=[jax.ShapeDtypeStruct((s, ch), F32), jax.ShapeDtypeStruct((1, ch), F32),
                   jax.ShapeDtypeStruct((1, ch), F32)],
        compiler_params=_params(),
    )(du3, uc, ln_w, ln_b)


def _conv_bwd(duc, u0, proj, conv_w, col0, name, rider=None):
    s = proj.shape[0]
    ch = conv_w.shape[1]
    nblk = ch // LANES
    a0 = col0 // LANES
    tr = 256
    half = (CONV_WIDTH - 1) // 2
    shift = CONV_PAD - half

    def body(duc_ref, u0_ref, a_ref, b_ref, w_ref, da_ref, db_ref, dw_ref, dbias_ref, pad_d, pad_u):
        z = jnp.zeros((CONV_PAD, LANES), F32)
        for buf in (pad_d, pad_u):
            buf[0:CONV_PAD, :] = z
            buf[s + CONV_PAD: s + 2 * CONV_PAD, :] = z

        def fill(rows):
            dst = pl.ds(rows.start + CONV_PAD, rows.size)
            pad_d[dst, :] = duc_ref[rows, :]
            pad_u[dst, :] = u0_ref[rows, :]

        _row_chunks(s, fill)
        dw_acc = [jnp.zeros((8, LANES), F32) for _ in range(CONV_WIDTH)]
        dbias_acc = jnp.zeros((8, LANES), F32)
        for t in range(0, s, tr):
            d_t = duc_ref[t:t + tr, :]
            dbias_acc = dbias_acc + jnp.sum(d_t.reshape(tr // 8, 8, LANES), axis=0)
            du0 = jnp.zeros((tr, LANES), F32)
            for k in range(CONV_WIDTH):
                du0 = du0 + w_ref[k:k + 1, :] * pad_d[t - k + half + CONV_PAD: t - k + half + CONV_PAD + tr, :]
                prod = d_t * pad_u[t + k + shift: t + k + shift + tr, :]
                dw_acc[k] = dw_acc[k] + jnp.sum(prod.reshape(tr // 8, 8, LANES), axis=0)
            av = a_ref[t:t + tr, :]
            sg = jax.nn.sigmoid(b_ref[t:t + tr, :])
            da_ref[t:t + tr, :] = (du0 * sg).astype(BF16)
            db_ref[t:t + tr, :] = (du0 * av * sg * (1.0 - sg)).astype(BF16)
        for k in range(CONV_WIDTH):
            dw_ref[k:k + 1, :] = jnp.sum(dw_acc[k], axis=0, keepdims=True)
        dbias_ref[...] = jnp.sum(dbias_acc, axis=0, keepdims=True)

    col = lambda off: pl.BlockSpec((s, LANES), lambda c: (0, off + c))
    return _pallas(
        body, name=name, grid=(nblk,),
        in_specs=[col(0), col(0), col(a0), col(a0 + nblk),
                  pl.BlockSpec((CONV_WIDTH, LANES), lambda c: (0, c))],
        out_specs=[col(0), col(0), pl.BlockSpec((CONV_WIDTH, LANES), lambda c: (0, c)),
                   pl.BlockSpec((1, LANES), lambda c: (0, c))],
        out_shape=[jax.ShapeDtypeStruct((s, ch), BF16)] * 2
        + [jax.ShapeDtypeStruct((CONV_WIDTH, ch), F32), jax.ShapeDtypeStruct((1, ch), F32)],
        operands=[duc, u0, proj, proj, conv_w],
        scratch_shapes=[pltpu.VMEM((s + 2 * CONV_PAD, LANES), F32)] * 2, rider=rider)


GATE_BLK = 512


def _gate_fwd(proj, bg, y_a, y_b, col0, name):
    s, d = y_a.shape
    tm = 256
    g0 = col0 // GATE_BLK
    nb = d // GATE_BLK

    def body(ga_ref, gb_ref, ba_ref, bb_ref, ya_ref, yb_ref, o_ref):
        g_a = jax.nn.sigmoid(ga_ref[...] + ba_ref[...])
        g_b = jax.nn.sigmoid(gb_ref[...] + bb_ref[...])
        o_ref[...] = (g_a * ya_ref[...] + g_b * yb_ref[...]).astype(BF16)

    act = pl.BlockSpec((tm, GATE_BLK), lambda i, j: (i, j))
    return pl.pallas_call(
        body, name=name, grid=(s // tm, nb),
        in_specs=[pl.BlockSpec((tm, GATE_BLK), lambda i, j: (i, g0 + j)),
                  pl.BlockSpec((tm, GATE_BLK), lambda i, j: (i, g0 + nb + j)),
                  pl.BlockSpec((None, 1, GATE_BLK), lambda i, j: (0, 0, j)),
                  pl.BlockSpec((None, 1, GATE_BLK), lambda i, j: (1, 0, j)), act, act],
        out_specs=act, out_shape=jax.ShapeDtypeStruct((s, d), BF16), compiler_params=_params(),
    )(proj, proj, bg, bg, y_a, y_b)


def _out_proj_bwd_gates(dx1, w_out, proj, bg, y_a, y_b, col0, name, after=()):
    s, d = y_a.shape
    tm = 256
    half = d // 2
    assert col0 % half == 0
    c0 = col0 // half
    nt_dims = (((1,), (1,)), ((), ()))

    def body(dx_ref, w_ref, a0_ref, a1_ref, b0_ref, b1_ref, bias_ref, ya_ref, yb_ref,
             dgl_ref, dya_ref, dyb_ref, db_ref):
        dm = lax.dot_general(dx_ref[...], w_ref[...], nt_dims, preferred_element_type=F32)
        parts = []
        for br, (lo_ref, hi_ref, y_ref, dy_ref) in enumerate(((a0_ref, a1_ref, ya_ref, dya_ref),
                                                              (b0_ref, b1_ref, yb_ref, dyb_ref))):
            logits = jnp.concatenate([lo_ref[...], hi_ref[...]], axis=1)
            gate = jax.nn.sigmoid(logits + bias_ref[br])
            dy_ref[...] = (dm * gate).astype(BF16)
            dgl = dm * y_ref[...] * gate * (1.0 - gate)
            dgl_ref[:, br * d:(br + 1) * d] = dgl.astype(BF16)
            parts.append(jnp.sum(dgl, axis=0, keepdims=True))
        part = jnp.concatenate(parts, axis=0)
        first = pl.program_id(0) == 0

        @pl.when(first)
        def _():
            db_ref[...] = part

        @pl.when(jnp.logical_not(first))
        def _():
            db_ref[...] += part

    row = pl.BlockSpec((tm, d), lambda i: (i, 0))
    logit_blk = lambda k: pl.BlockSpec((tm, half), functools.partial(lambda i, k: (i, c0 + k), k=k))
    return _pallas(
        body, name=name, grid=(s // tm,),
        in_specs=[row, pl.BlockSpec((d, d), lambda i: (0, 0)), logit_blk(0), logit_blk(1), logit_blk(2),
                  logit_blk(3), pl.BlockSpec((2, 1, d), lambda i: (0, 0, 0)), row, row],
        out_specs=[pl.BlockSpec((tm, 2 * d), lambda i: (i, 0)), row, row, pl.BlockSpec((2, d), lambda i: (0, 0))],
        out_shape=[jax.ShapeDtypeStruct((s, 2 * d), BF16), jax.ShapeDtypeStruct((s, d), BF16),
                   jax.ShapeDtypeStruct((s, d), BF16), jax.ShapeDtypeStruct((2, d), F32)],
        operands=[dx1, w_out, proj, proj, proj, proj, bg, y_a, y_b], after=after)


def _ffn_in_swiglu(h2, w_blocked, name):
    s, k = h2.shape
    nblk, _, tn = w_blocked.shape
    ff = nblk // 2 * tn
    tm = 512

    def body(a_ref, wg_ref, wu_ref, g_ref, u_ref, act_ref):
        a = a_ref[...]
        gt = jnp.dot(a, wg_ref[...], preferred_element_type=F32)
        up = jnp.dot(a, wu_ref[...], preferred_element_type=F32)
        g_ref[...] = gt
        u_ref[...] = up
        act_ref[...] = (gt * jax.nn.sigmoid(gt) * up).astype(BF16)

    out = pl.BlockSpec((tm, tn), lambda j, i: (i, j))
    return pl.pallas_call(
        body, name=name, grid=(nblk // 2, s // tm),
        in_specs=[pl.BlockSpec((tm, k), lambda j, i: (i, 0)),
                  pl.BlockSpec((None, k, tn), lambda j, i: (j, 0, 0)),
                  pl.BlockSpec((None, k, tn), lambda j, i: (nblk // 2 + j, 0, 0))],
        out_specs=[out, out, out],
        out_shape=[jax.ShapeDtypeStruct((s, ff), F32), jax.ShapeDtypeStruct((s, ff), F32),
                   jax.ShapeDtypeStruct((s, ff), BF16)],
        compiler_params=_params(),
    )(h2, w_blocked, w_blocked)


def _ffn_out_bwd_swiglu(dy, w_ffn_out, gate, up, name, rider=None):
    s, d = dy.shape
    ff = gate.shape[1]
    tm = 256
    nt_dims = (((1,), (1,)), ((), ()))

    def body(dy_ref, w_ref, g_ref, u_ref, o_ref):
        dv = lax.dot_general(dy_ref[...], w_ref[...], nt_dims, preferred_element_type=F32)
        gt = g_ref[...]
        sg = jax.nn.sigmoid(gt)
        o_ref[:, 0:ff] = (dv * u_ref[...] * (sg * (1.0 + gt * (1.0 - sg)))).astype(BF16)
        o_ref[:, ff:2 * ff] = (dv * gt * sg).astype(BF16)

    row = pl.BlockSpec((tm, ff), lambda i: (i, 0))
    return _pallas(
        body, name=name, grid=(s // tm,),
        in_specs=[pl.BlockSpec((tm, d), lambda i: (i, 0)), pl.BlockSpec((ff, d), lambda i: (0, 0)), row, row],
        out_specs=pl.BlockSpec((tm, 2 * ff), lambda i: (i, 0)),
        out_shape=jax.ShapeDtypeStruct((s, 2 * ff), BF16), operands=[dy, w_ffn_out, gate, up], rider=rider)


def _out_proj_rmsnorm(mixed, w_out, x, norm_w, name):
    s, k = mixed.shape
    d = w_out.shape[1]
    tm = 512

    def body(a_ref, w_ref, x_ref, nw_ref, x1_ref, h2_ref):
        x1 = x_ref[...] + jnp.dot(a_ref[...], w_ref[...], preferred_element_type=F32)
        x1_ref[...] = x1
        rstd = lax.rsqrt(jnp.mean(x1 * x1, axis=-1, keepdims=True) + EPS)
        h2_ref[...] = (x1 * rstd * nw_ref[...]).astype(BF16)

    row = pl.BlockSpec((tm, d), lambda i: (i, 0))
    return pl.pallas_call(
        body, name=name, grid=(s // tm,),
        in_specs=[pl.BlockSpec((tm, k), lambda i: (i, 0)), pl.BlockSpec((k, d), lambda i: (0, 0)), row,
                  pl.BlockSpec((1, d), lambda i: (0, 0))],
        out_specs=[row, row],
        out_shape=[jax.ShapeDtypeStruct((s, d), F32), jax.ShapeDtypeStruct((s, d), BF16)],
        compiler_params=_params(),
    )(mixed, w_out, x, norm_w)


def _ffn_out_loss(act, w_ffn_out, x1, target, name):
    s, k = act.shape
    d = w_ffn_out.shape[1]
    tm = 512

    def body(a_ref, w_ref, x1_ref, t_ref, dy_ref, dyb_ref, loss_ref, acc):
        y = x1_ref[...] + jnp.dot(a_ref[...], w_ref[...], preferred_element_type=F32)
        diff = y - t_ref[...]
        dy = diff * (1.0 / d)
        dy_ref[...] = dy
        dyb_ref[...] = dy.astype(BF16)
        part = jnp.sum((diff * diff).reshape(tm // 8, 8, d), axis=0)
        i = pl.program_id(0)

        @pl.when(i == 0)
        def _():
            acc[...] = part

        @pl.when(i > 0)
        def _():
            acc[...] += part

        @pl.when(i == pl.num_programs(0) - 1)
        def _():
            loss_ref[...] = (0.5 / d) * jnp.sum(jnp.sum(acc[...], axis=1, keepdims=True), axis=0, keepdims=True)

    row = pl.BlockSpec((tm, d), lambda i: (i, 0))
    return pl.pallas_call(
        body, name=name, grid=(s // tm,),
        in_specs=[pl.BlockSpec((tm, k), lambda i: (i, 0)), pl.BlockSpec((k, d), lambda i: (0, 0)), row, row],
        out_specs=[row, row, pl.BlockSpec((1, 1), lambda i: (0, 0))],
        out_shape=[jax.ShapeDtypeStruct((s, d), F32), jax.ShapeDtypeStruct((s, d), BF16),
                   jax.ShapeDtypeStruct((1, 1), F32)],
        scratch_shapes=[pltpu.VMEM((8, d), F32)], compiler_params=_params(),
    )(act, w_ffn_out, x1, target)


LATE_GATHER = ("w_o_attn", "w_pw_conv", "w_out", "w_ffn_in", "w_ffn_out")
EARLY_REDUCE = LATE_GATHER


def _blocks_by_half(g):
    if g.ndim == 2:
        g = g.reshape(N_CHIPS, g.shape[0] // N_CHIPS, g.shape[1])
    return g.reshape(N_CHIPS, 2, g.shape[1] // 2, g.shape[2])


def _forward_backward(x, pos_col, target, wts, first_gather, late_bufs, pos_arr):
    wts = dict(wts)
    consts = _rope_consts()
    bd = consts[3]
    qw2 = jnp.tile(wts["q_norm_w"], (1, LANES // HEAD_DIM))
    kw2 = jnp.tile(wts["k_norm_w"], (1, LANES // HEAD_DIM))
    qkv_w = 3 * N_SLOT_HEADS * HEAD_DIM
    conv_col0 = 3 * qkv_w

    h = _rmsnorm_fwd(x, wts["norm1_w"], "rms1_fwd")
    first_gather = _split_middle(first_gather, after=[h] + list(late_bufs), name="allgather_w_in_forward")
    (w_in_buf, conv_w_buf, b_gate_buf), _ = _split_wait(first_gather, after=[], name="allgather_w_in_wait")
    wts["w_in"] = w_in_buf.reshape(N_CHIPS, -1, w_in_buf.shape[3])
    wts["conv_w"] = conv_w_buf.transpose(1, 0, 2).reshape(CONV_WIDTH, -1)
    wts["b_gate"] = b_gate_buf.transpose(1, 0, 2).reshape(2, 1, -1)
    ch = wts["conv_w"].shape[1]
    gate_col0 = conv_col0 + 2 * ch
    gathering, started = _split_start(_gather_ici_rider(late_bufs, []), "late_gather_start", after=[wts["w_in"]])
    proj = _matmul(h, wts["w_in"], mode="nn", tm=512, tn=1920, tk=1024, out_dtype=F32, name="mm_proj",
                   b_blocked=True, cols_outer=True, after=[started])
    qn, kn = _qk_fwd(proj, pos_col, qw2, kw2, consts, "qk_fwd")
    attn, lse, attn_b = _attn_fwd(qn, kn, proj, "attn_fwd")
    late_bufs, _ = _split_wait(gathering, after=[attn_b], name="late_gather_wait")
    (u0, uc), late_bufs = _conv_fwd(proj, wts["conv_w"], wts["conv_b"], conv_col0, "conv_fwd",
                                    rider=_gather_forward_rider(late_bufs))
    for n, buf in zip(LATE_GATHER, late_bufs):
        full = buf.reshape(N_CHIPS, -1, buf.shape[3])
        wts[n] = full.reshape(-1, full.shape[2]) if n in ROW_SHARDED else full
    y_a = _matmul(attn_b, wts["w_o_attn"], mode="nn", tm=1024, tn=256, tk=512, out_dtype=F32, name="mm_ya",
                  b_blocked=True)
    u3 = _ln_silu_fwd(uc, wts["conv_ln_w"], wts["conv_ln_b"], "ln_fwd")
    y_b = _matmul(u3, wts["w_pw_conv"], mode="nn", tm=1024, tn=256, tk=512, out_dtype=F32, name="mm_yb",
                  b_blocked=True)
    mixed = _gate_fwd(proj, wts["b_gate"], y_a, y_b, gate_col0, "gate_fwd")
    x1, h2 = _out_proj_rmsnorm(mixed, wts["w_out"], x, wts["norm2_w"], "mm_x1_rms2")
    gate, up, act = _ffn_in_swiglu(h2, wts["w_ffn_in"], "mm_gu_swiglu")
    dy, dy_b16, loss = _ffn_out_loss(act, wts["w_ffn_out"], x1, target, "mm_x2_loss")

    g = {}
    by_chip = {}

    def pair_add(n, blocks, received):
        return _add_own_half(blocks, received, pos_arr, f"grads_pair_add_{n}")

    g_ffn_out = _blocks_by_half(
        _matmul(act, dy_b16, mode="tn", tm=1408, tn=1024, tk=2048, out_dtype=F32, name="mm_dwffnout"))
    dgu, (received,) = _ffn_out_bwd_swiglu(dy_b16, wts["w_ffn_out"], gate, up, "mm_dact_swiglu_bwd",
                                           rider=_pair_exchange_rider([g_ffn_out], halved=True))
    to_send, own = pair_add("w_ffn_out", g_ffn_out, received)
    dh2, (by_chip["w_ffn_out"],) = _matmul(
        dgu, wts["w_ffn_in"], mode="nt", tm=1024, tn=1024, tk=1408, out_dtype=F32, name="mm_dh2", b_blocked=True,
        rider=_chip_exchange_rider([to_send], [own]))
    g_ffn_in = _blocks_by_half(_matmul(h2, dgu, mode="tn", tm=512, tn=1408, tk=2048, out_dtype=F32,
                                       name="mm_dwffnin", out_blocked=N_CHIPS, cols_outer=True))
    exchanging, started = _split_start(_pair_exchange_rider([g_ffn_in], halved=True), "grads_ffn_in_pair_start")
    dx1, dx1_b16, g["norm2_w"] = _rmsnorm_bwd(dh2, x1, wts["norm2_w"], dy, "rms2_bwd")
    g["w_out"] = _matmul(mixed, dx1_b16, mode="tn", tm=512, tn=1024, tk=2048, out_dtype=F32, name="mm_dwout",
                         after=[started])
    dgl, dy_a, dy_b, g["b_gate"] = _out_proj_bwd_gates(dx1_b16, wts["w_out"], proj, wts["b_gate"], y_a, y_b,
                                                       gate_col0, "mm_dmixed_gate_bwd")
    (received,), (g_ffn_in,) = _split_wait(exchanging, after=[dgl], name="grads_ffn_in_pair_wait")
    ffn_in_to_send, ffn_in_own = pair_add("w_ffn_in", g_ffn_in, received)
    dattn = _matmul(dy_a, wts["w_o_attn"], mode="nt", tm=1024, tn=512, tk=256, out_dtype=F32, name="mm_dattn",
                    b_blocked=True)
    g["w_o_attn"] = _matmul(attn_b, dy_a, mode="tn", tm=512, tn=256, tk=2048, out_dtype=F32, name="mm_dwo",
                            out_blocked=N_CHIPS)
    du3 = _matmul(dy_b, wts["w_pw_conv"], mode="nt", tm=1024, tn=512, tk=256, out_dtype=F32, name="mm_du3",
                  b_blocked=True)
    g["w_pw_conv"] = _matmul(u3, dy_b, mode="tn", tm=512, tn=256, tk=2048, out_dtype=F32, name="mm_dwpw",
                             out_blocked=N_CHIPS)
    duc, g["conv_ln_w"], g["conv_ln_b"] = _ln_silu_bwd(du3, uc, wts["conv_ln_w"], wts["conv_ln_b"], "ln_bwd")

    small3 = ("w_out", "w_o_attn", "w_pw_conv")
    g_small3 = [_blocks_by_half(g.pop(n)) for n in small3]
    (da, db, g["conv_w"], g["conv_b"]), received = _conv_bwd(
        duc, u0, proj, wts["conv_w"], conv_col0, "conv_bwd", rider=_pair_exchange_rider(g_small3, halved=True))
    sums3 = [pair_add(n, gb, rv) for n, gb, rv in zip(small3, g_small3, received)]
    (dqn, dkn, dv), (by_chip["w_ffn_in"],) = _attn_bwd(
        qn, kn, proj, dattn, attn, lse, bd, "attn_bwd",
        rider=_chip_exchange_rider([ffn_in_to_send], [ffn_in_own]))
    (dproj, dqw, dkw), exchanged3 = _qk_bwd(
        dqn, dkn, dv, da, db, dgl, proj, pos_col, qw2, kw2, consts, "qk_bwd",
        rider=_chip_exchange_rider([s[0] for s in sums3], [s[1] for s in sums3]))
    by_chip.update(zip(small3, exchanged3))
    halves = [_sum_chips(by_chip[n], pos_arr, f"grads_chip_sum_{n}") for n in EARLY_REDUCE]
    g["q_norm_w"] = dqw[:, :HEAD_DIM]
    g["k_norm_w"] = dkw[:, :HEAD_DIM]

    c = pos_arr[0]
    rh = h.shape[1] // 2
    h_sibling = lax.dynamic_slice_in_dim(h, (1 - c) * rh, rh, axis=1)
    h_own = lax.dynamic_slice_in_dim(h, c * rh, rh, axis=1)
    g_sibling, shards = _matmul(h_sibling, dproj, mode="tn", tm=rh, tn=1920, tk=2048, out_dtype=F32,
                                name="mm_dwin_sibling", out_blocked=N_CHIPS, rider=_pair_gather_rider(halves))
    reduced = dict(zip(EARLY_REDUCE, shards))
    exchanging, started = _split_start(_pair_exchange_rider([g_sibling], halved=False), "grads_w_in_pair_start")
    g_own = _matmul(h_own, dproj, mode="tn", tm=rh, tn=1920, tk=2048, out_dtype=F32, name="mm_dwin_own",
                    out_blocked=N_CHIPS, after=[started])
    (from_sibling,), _ = _split_wait(exchanging, after=[g_own], name="grads_w_in_pair_wait")
    to_send, own = _add_own_half(g_own, from_sibling, pos_arr, "grads_pair_add_w_in")
    in_flight, started = _split_start(_chip_exchange_rider([to_send], [own]), "grads_w_in_exchange_start")
    dh = _matmul(dproj, wts["w_in"], mode="nt", tm=1024, tn=1024, tk=1920, out_dtype=F32, name="mm_dh",
                 b_blocked=True, after=[started])
    grad_x, _, g["norm1_w"] = _rmsnorm_bwd(dh, x, wts["norm1_w"], dx1, "rms1_bwd")
    return loss, grad_x, g, reduced, (in_flight, started)


def _mesh_pos():
    return lax.axis_index("x"), lax.axis_index("y"), lax.axis_index("c")


def _other_chips(x, y):
    return [(1 - x, y), (x, 1 - y), (1 - x, 1 - y)]


def _cast_into_slot(shard, chip_arr, dtype, name, n_slots=N_CHIPS, after=()):
    r, c = shard.shape
    tr = r // 2 if r % 32 == 0 else r

    def body(chip_ref, s_ref, *refs):
        refs[-1][...] = s_ref[...].astype(dtype)

    return pl.pallas_call(
        body, name=name,
        grid_spec=pltpu.PrefetchScalarGridSpec(
            num_scalar_prefetch=1, grid=(r // tr,),
            in_specs=[pl.BlockSpec((tr, c), lambda i, chip_ref: (i, 0))] + [ANY] * len(after),
            out_specs=pl.BlockSpec((None, tr, c), lambda i, chip_ref: (chip_ref[0], i, 0))),
        out_shape=jax.ShapeDtypeStruct((n_slots, r, c), dtype), compiler_params=_params(),
    )(chip_arr, shard, *after)


GATHER_CHUNKS = 4


def _gather_both_legs_rider(big, small):
    nb = len(big)
    n = nb + len(small)
    nch = GATHER_CHUNKS

    def part(bufs, a, slot, half, ch):
        if a >= nb:
            return bufs[a].at[slot]
        rows = bufs[a].shape[2] // nch
        return bufs[a].at[slot, half, pl.ds(ch * rows, rows)]

    def pieces():
        return [(a, ch, k) for ch in range(nch) for a in range(n) for k in range(3) if a < nb or ch == 0]

    def ici(bufs, sems, a, ch, k, slot_of_src):
        x, y, c = _mesh_pos()
        px, py = _other_chips(x, y)[k]
        slot = 2 * x + y if slot_of_src == "mine" else 2 * px + py
        return pltpu.make_async_remote_copy(
            src_ref=part(bufs, a, slot, c, ch), dst_ref=part(bufs, a, slot, c, ch), send_sem=sems[0].at[a, ch, k],
            recv_sem=sems[1].at[a, ch, k], device_id=(px, py, c), device_id_type=MESH)

    def forward(bufs, sems, a, ch, k, half):
        x, y, c = _mesh_pos()
        px, py = _other_chips(x, y)[k]
        h = c if half == "mine" else 1 - c
        return pltpu.make_async_remote_copy(
            src_ref=part(bufs, a, 2 * px + py, h, ch), dst_ref=part(bufs, a, 2 * px + py, h, ch),
            send_sem=sems[2].at[a, ch, k], recv_sem=sems[3].at[a, ch, k], device_id=(x, y, 1 - c),
            device_id_type=MESH)

    def start(r_in, bufs, sems):
        for a, ch, k in pieces():
            ici(bufs, sems, a, ch, k, "mine").start()

    def middle(r_in, bufs, sems):
        for a, ch, k in pieces():
            ici(bufs, sems, a, ch, k, "theirs").wait_recv()
            if a < nb:
                forward(bufs, sems, a, ch, k, "mine").start()
        for a, ch, k in pieces():
            ici(bufs, sems, a, ch, k, "mine").wait_send()

    def wait(r_in, bufs, sems):
        for a, ch, k in pieces():
            if a < nb:
                forward(bufs, sems, a, ch, k, "theirs").wait_recv()
        for a, ch, k in pieces():
            if a < nb:
                forward(bufs, sems, a, ch, k, "mine").wait_send()

    ops = list(big) + list(small)
    return _Rider(ops, [jax.ShapeDtypeStruct(o.shape, o.dtype) for o in ops], {i: i for i in range(n)},
                  [pltpu.SemaphoreType.DMA((n, nch, 3)), pltpu.SemaphoreType.DMA((n, nch, 3)),
                   pltpu.SemaphoreType.DMA((nb, nch, 3)), pltpu.SemaphoreType.DMA((nb, nch, 3))],
                  start, wait, middle)


def _comm_call(rider, name):
    def body():
        pass

    return _pallas(body, name=name, grid=(1,), in_specs=[], out_specs=[], out_shape=[], operands=[],
                   rider=rider)[1]


def _gather_ici_rider(big, small):
    nb = len(big)
    n = nb + len(small)

    def copies(bufs, sems):
        x, y, c = _mesh_pos()
        me = 2 * x + y
        part = lambda a, slot: bufs[a].at[slot, c] if a < nb else bufs[a].at[slot]
        out = []
        for a in range(n):
            for k, (px, py) in enumerate(_other_chips(x, y)):
                send = functools.partial(
                    pltpu.make_async_remote_copy,
                    src_ref=part(a, me), dst_ref=part(a, me), send_sem=sems[0].at[a, k],
                    recv_sem=sems[1].at[a, k], device_id=(px, py, c), device_id_type=MESH)
                recv = functools.partial(
                    pltpu.make_async_remote_copy,
                    src_ref=part(a, 2 * px + py), dst_ref=part(a, 2 * px + py), send_sem=sems[0].at[a, k],
                    recv_sem=sems[1].at[a, k], device_id=(px, py, c), device_id_type=MESH)
                out.append((send, recv))
        return out

    def start(r_in, r_out, sems):
        for send, _ in copies(r_out, sems):
            send().start()

    def wait(r_in, r_out, sems):
        cps = copies(r_out, sems)
        for _, recv in cps:
            recv().wait_recv()
        for send, _ in cps:
            send().wait_send()

    ops = list(big) + list(small)
    return _Rider(ops, [jax.ShapeDtypeStruct(o.shape, o.dtype) for o in ops], {i: i for i in range(n)},
                  [pltpu.SemaphoreType.DMA((n, 3)), pltpu.SemaphoreType.DMA((n, 3))], start, wait)


def _gather_forward_rider(big):
    n = len(big)

    def copies(bufs, sems):
        x, y, c = _mesh_pos()
        out = []
        for a in range(n):
            for k, (px, py) in enumerate(_other_chips(x, y)):
                slot = 2 * px + py
                send = functools.partial(
                    pltpu.make_async_remote_copy,
                    src_ref=bufs[a].at[slot, c], dst_ref=bufs[a].at[slot, c], send_sem=sems[0].at[a, k],
                    recv_sem=sems[1].at[a, k], device_id=(x, y, 1 - c), device_id_type=MESH)
                recv = functools.partial(
                    pltpu.make_async_remote_copy,
                    src_ref=bufs[a].at[slot, 1 - c], dst_ref=bufs[a].at[slot, 1 - c], send_sem=sems[0].at[a, k],
                    recv_sem=sems[1].at[a, k], device_id=(x, y, 1 - c), device_id_type=MESH)
                out.append((send, recv))
        return out

    def start(r_in, r_out, sems):
        for send, _ in copies(r_out, sems):
            send().start()

    def wait(r_in, r_out, sems):
        cps = copies(r_out, sems)
        for _, recv in cps:
            recv().wait_recv()
        for send, _ in cps:
            send().wait_send()

    return _Rider(big, [jax.ShapeDtypeStruct(o.shape, o.dtype) for o in big], {i: i for i in range(n)},
                  [pltpu.SemaphoreType.DMA((n, 3)), pltpu.SemaphoreType.DMA((n, 3))], start, wait)


def _pair_exchange_rider(gs, halved):
    n = len(gs)

    def copies(r_in, r_out, sems):
        x, y, c = _mesh_pos()
        return [pltpu.make_async_remote_copy(
            src_ref=r_in[a].at[:, 1 - c] if halved else r_in[a], dst_ref=r_out[a], send_sem=sems[0].at[a],
            recv_sem=sems[1].at[a], device_id=(x, y, 1 - c), device_id_type=MESH) for a in range(n)]

    def start(r_in, r_out, sems):
        for cp in copies(r_in, r_out, sems):
            cp.start()

    def wait(r_in, r_out, sems):
        for cp in copies(r_in, r_out, sems):
            cp.wait()

    return _Rider(gs, [jax.ShapeDtypeStruct((g.shape[0],) + g.shape[-2:], g.dtype) for g in gs], {},
                  [pltpu.SemaphoreType.DMA((n,)), pltpu.SemaphoreType.DMA((n,))], start, wait)


def _chip_exchange_rider(to_send, by_chip, row_range=None):
    n = len(to_send)

    def copies(r_in, r_out, sems):
        x, y, c = _mesh_pos()
        me = 2 * x + y
        rows = (lambda ref: ref) if row_range is None else (lambda ref: ref.at[pl.ds(*row_range)])
        out = []
        for a in range(n):
            for k, (px, py) in enumerate(_other_chips(x, y)):
                send = functools.partial(
                    pltpu.make_async_remote_copy,
                    src_ref=rows(r_in[a].at[2 * px + py]), dst_ref=rows(r_out[a].at[me]),
                    send_sem=sems[0].at[a, k], recv_sem=sems[1].at[a, k], device_id=(px, py, c),
                    device_id_type=MESH)
                recv = functools.partial(
                    pltpu.make_async_remote_copy,
                    src_ref=rows(r_in[a].at[me]), dst_ref=rows(r_out[a].at[2 * px + py]),
                    send_sem=sems[0].at[a, k], recv_sem=sems[1].at[a, k], device_id=(px, py, c),
                    device_id_type=MESH)
                out.append((send, recv))
        return out

    def start(r_in, r_out, sems):
        for send, _ in copies(r_in, r_out, sems):
            send().start()

    def wait(r_in, r_out, sems):
        cps = copies(r_in, r_out, sems)
        for _, recv in cps:
            recv().wait_recv()
        for send, _ in cps:
            send().wait_send()

    return _Rider(list(to_send) + list(by_chip), [jax.ShapeDtypeStruct(b.shape, b.dtype) for b in by_chip],
                  {n + i: i for i in range(n)},
                  [pltpu.SemaphoreType.DMA((n, 3)), pltpu.SemaphoreType.DMA((n, 3))], start, wait)


HBM = pl.BlockSpec(memory_space=pltpu.HBM)
SEM = pl.BlockSpec(memory_space=pltpu.SEMAPHORE)


_IN_FLIGHT = pltpu.CompilerParams(has_side_effects=pltpu.SideEffectType.DATAFLOW_SIDE_EFFECTING)


class _FlatSems:
    def __init__(self, ref, shape):
        self.ref, self.shape = ref, shape

    @property
    def at(self):
        return self

    def __getitem__(self, idx):
        idx = idx if isinstance(idx, tuple) else (idx,)
        flat = 0
        for i, n in zip(idx, self.shape):
            flat = flat * n + i
        return self.ref.at[flat]


def _flat_sem_types(rider):
    return tuple(pltpu.SemaphoreType.DMA((int(np.prod(s.shape)),)) for s in rider.scratch)


def _as_rider_sems(rider, refs):
    return [_FlatSems(r, s.shape) for r, s in zip(refs, rider.scratch)]


def _split_start(rider, name, after=()):
    n_in, n_out, n_sem = len(rider.operands), len(rider.out_shapes), len(rider.scratch)
    n_after = len(after)
    fresh = [j for j in range(n_out) if j not in rider.aliases.values()]
    by_out = {j: i for i, j in rider.aliases.items()}

    def body(*refs):
        r_in = refs[:n_in]
        refs = refs[n_in + n_after:]
        sems = refs[:n_sem]
        thru = refs[n_sem:n_sem + n_in]
        fresh_refs = refs[n_sem + n_in:n_sem + n_in + len(fresh)]
        token = refs[-1]
        r_out = [thru[by_out[j]] if j in by_out else fresh_refs[fresh.index(j)] for j in range(n_out)]
        rider.start(r_in, r_out, _as_rider_sems(rider, sems))
        token[...] = jnp.zeros_like(token)

    res = pl.pallas_call(
        body, name=name,
        out_shape=_flat_sem_types(rider) + tuple(pltpu.HBM(o.shape, o.dtype) for o in rider.operands)
        + tuple(pltpu.HBM(rider.out_shapes[j].shape, rider.out_shapes[j].dtype) for j in fresh)
        + (jax.ShapeDtypeStruct((8, LANES), F32),),
        in_specs=(HBM,) * n_in + (ANY,) * n_after,
        out_specs=(SEM,) * n_sem + (HBM,) * (n_in + len(fresh)) + (pl.BlockSpec(memory_space=pltpu.VMEM),),
        input_output_aliases={i: n_sem + i for i in range(n_in)}, compiler_params=_IN_FLIGHT,
    )(*[pltpu.with_memory_space_constraint(o, pltpu.HBM) for o in rider.operands], *after)
    return (rider, res[:n_sem], res[n_sem:n_sem + n_in], res[n_sem + n_in:-1]), res[-1]


def _split_continue(handles, after, name, phase):
    rider, sems, thru, fresh_arrays = handles
    n_in, n_out, n_sem = len(rider.operands), len(rider.out_shapes), len(rider.scratch)
    fresh = [j for j in range(n_out) if j not in rider.aliases.values()]
    by_out = {j: i for i, j in rider.aliases.items()}
    n_data = n_in + len(fresh)

    def body(*refs):
        r_in = refs[:n_in]
        fresh_refs = refs[n_in:n_data]
        sem_refs = refs[n_data:n_data + n_sem]
        r_out = [r_in[by_out[j]] if j in by_out else fresh_refs[fresh.index(j)] for j in range(n_out)]
        phase(r_in, r_out, _as_rider_sems(rider, sem_refs))

    data = list(thru) + list(fresh_arrays)
    return pl.pallas_call(
        body, name=name, out_shape=tuple(pltpu.HBM(d.shape, d.dtype) for d in data),
        in_specs=(HBM,) * n_data + (SEM,) * n_sem + (ANY,) * len(after), out_specs=(HBM,) * n_data,
        input_output_aliases={i: i for i in range(n_data)}, compiler_params=_IN_FLIGHT,
    )(*data, *sems, *after)


def _split_middle(handles, after, name):
    rider, sems, thru, _ = handles
    res = _split_continue(handles, after, name, rider.middle)
    return rider, sems, res[:len(thru)], res[len(thru):]


def _split_wait(handles, after, name):
    rider = handles[0]
    n_in, n_out = len(rider.operands), len(rider.out_shapes)
    fresh = [j for j in range(n_out) if j not in rider.aliases.values()]
    by_out = {j: i for i, j in rider.aliases.items()}
    res = _split_continue(handles, after, name, rider.wait)
    return [res[by_out[j]] if j in by_out else res[n_in + fresh.index(j)] for j in range(n_out)], res[:n_in]


def _pair_gather_rider(bufs):
    n = len(bufs)

    def copies(r_out, sems):
        x, y, c = _mesh_pos()
        out = []
        for a in range(n):
            send = functools.partial(
                    pltpu.make_async_remote_copy,
                src_ref=r_out[a].at[c], dst_ref=r_out[a].at[c], send_sem=sems[0].at[a],
                recv_sem=sems[1].at[a], device_id=(x, y, 1 - c), device_id_type=MESH)
            recv = functools.partial(
                    pltpu.make_async_remote_copy,
                src_ref=r_out[a].at[1 - c], dst_ref=r_out[a].at[1 - c], send_sem=sems[0].at[a],
                recv_sem=sems[1].at[a], device_id=(x, y, 1 - c), device_id_type=MESH)
            out.append((send, recv))
        return out

    def start(r_in, r_out, sems):
        for send, _ in copies(r_out, sems):
            send().start()

    def wait(r_in, r_out, sems):
        cps = copies(r_out, sems)
        for _, recv in cps:
            recv().wait_recv()
        for send, _ in cps:
            send().wait_send()

    return _Rider(bufs, [jax.ShapeDtypeStruct(b.shape, b.dtype) for b in bufs], {i: i for i in range(n)},
                  [pltpu.SemaphoreType.DMA((n,)), pltpu.SemaphoreType.DMA((n,))], start, wait)


def _add_own_half(g, recv, pos_arr, name):
    nb, rh, cols = g.shape[0], g.shape[-2], g.shape[-1]

    def body(pos_ref, g_ref, r_ref, send_ref, own_ref):
        s = (g_ref[...] + r_ref[...]).astype(BF16)
        send_ref[...] = s

        @pl.when(pl.program_id(0) == pos_ref[1])
        def _():
            own_ref[...] = s

    blk = pl.BlockSpec((None, rh, cols), lambda j, pos_ref: (j, 0, 0))
    g_spec = blk if g.ndim == 3 else pl.BlockSpec((None, None, rh, cols),
                                                   lambda j, pos_ref: (j, pos_ref[0], 0, 0))
    shape = jax.ShapeDtypeStruct((nb, rh, cols), BF16)
    return pl.pallas_call(
        body, name=name,
        grid_spec=pltpu.PrefetchScalarGridSpec(
            num_scalar_prefetch=1, grid=(nb,), in_specs=[g_spec, blk],
            out_specs=[blk, pl.BlockSpec((None, rh, cols), lambda j, pos_ref: (pos_ref[1], 0, 0))]),
        out_shape=[shape, shape], compiler_params=_params(),
    )(pos_arr, g, recv)


def _sum_chips(gath, pos_arr, name):
    nb, rh, cols = gath.shape

    def body(pos_ref, a_ref, b_ref, c_ref, d_ref, o_ref):
        del pos_ref
        o_ref[...] = ((a_ref[...].astype(F32) + b_ref[...].astype(F32)) + c_ref[...].astype(F32)) \
            + d_ref[...].astype(F32)

    tr = rh // 2 if (rh // 2) % 16 == 0 else rh
    specs = [pl.BlockSpec((None, tr, cols), functools.partial(lambda i, pos_ref, j: (j, i, 0), j=j))
             for j in range(nb)]
    return pl.pallas_call(
        body, name=name,
        grid_spec=pltpu.PrefetchScalarGridSpec(
            num_scalar_prefetch=1, grid=(rh // tr,), in_specs=specs,
            out_specs=pl.BlockSpec((None, tr, cols), lambda i, pos_ref: (pos_ref[0], i, 0))),
        out_shape=jax.ShapeDtypeStruct((2, rh, cols), F32), compiler_params=_params(),
    )(pos_arr, gath, gath, gath, gath)


N_DEVICES = 8


def _small_gather_rider(buf):
    def copies(r_out, sems):
        x, y, c = _mesh_pos()
        me = 4 * x + 2 * y + c
        out = []
        for r in range(1, N_DEVICES):
            px = 1 - x if r & 4 else x
            py = 1 - y if r & 2 else y
            pc = 1 - c if r & 1 else c
            out.append(pltpu.make_async_remote_copy(
                src_ref=r_out[0].at[me], dst_ref=r_out[0].at[me], send_sem=sems[0].at[r - 1],
                recv_sem=sems[1].at[r - 1], device_id=(px, py, pc), device_id_type=MESH))
        return out

    def start(r_in, r_out, sems):
        for cp in copies(r_out, sems):
            cp.start()

    def wait(r_in, r_out, sems):
        cps = copies(r_out, sems)
        for cp in cps:
            cp.wait_recv()
        for cp in cps:
            cp.wait_send()

    return _Rider([buf], [jax.ShapeDtypeStruct(buf.shape, buf.dtype)], {0: 0},
                  [pltpu.SemaphoreType.DMA((N_DEVICES - 1,)), pltpu.SemaphoreType.DMA((N_DEVICES - 1,))],
                  start, wait)


def _sum_devices(buf, name):
    def body(b_ref, o_ref):
        acc = b_ref[0]
        for i in range(1, N_DEVICES):
            acc = acc + b_ref[i]
        o_ref[...] = acc

    return _pallas(body, name=name, grid=(1,), in_specs=[pl.BlockSpec(buf.shape, lambda i: (0, 0, 0))],
                   out_specs=pl.BlockSpec(buf.shape[1:], lambda i: (0, 0)),
                   out_shape=jax.ShapeDtypeStruct(buf.shape[1:], F32), operands=[buf])


def _adamw_math(w, g, m, v):
    m = ADAM_B1 * m + (1.0 - ADAM_B1) * g
    v = ADAM_B2 * v + (1.0 - ADAM_B2) * (g * g)
    m_hat = m / (1.0 - ADAM_B1 ** ADAM_STEP)
    v_hat = v / (1.0 - ADAM_B2 ** ADAM_STEP)
    delta = -ADAM_LR * (m_hat / (jnp.sqrt(v_hat) + ADAM_EPS) + ADAM_WD * w)
    return delta, m, v


def _adamw(w, g, m, v, name, after=()):
    r, c = w.shape
    tr = 128 if r % 128 == 0 else 64
    assert r % tr == 0

    def body(w_ref, g_ref, m_ref, v_ref, go_ref, d_ref, mo_ref, vo_ref):
        gv = g_ref[...]
        d, mn, vn = _adamw_math(w_ref[...], gv, m_ref[...], v_ref[...])
        go_ref[...] = gv
        d_ref[...] = d
        mo_ref[...] = mn
        vo_ref[...] = vn

    blk = pl.BlockSpec((tr, c), lambda i: (i, 0))
    return _pallas(body, name=name, grid=(r // tr,), in_specs=[blk] * 4, out_specs=[blk] * 4,
                   out_shape=[jax.ShapeDtypeStruct((r, c), F32)] * 4, operands=[w, g, m, v], after=after)


def _adamw_small(ws, gs, ms, vs, name):
    n = len(ws)

    def body(*refs):
        w_r, g_r, m_r, v_r = refs[:n], refs[n:2 * n], refs[2 * n:3 * n], refs[3 * n:4 * n]
        d_o, m_o, v_o = refs[4 * n:5 * n], refs[5 * n:6 * n], refs[6 * n:7 * n]
        for i in range(n):
            d, mn, vn = _adamw_math(w_r[i][...], g_r[i][...], m_r[i][...], v_r[i][...])
            d_o[i][...] = d
            m_o[i][...] = mn
            v_o[i][...] = vn

    specs = [pl.BlockSpec(w.shape, lambda i: (0, 0)) for w in ws]
    shapes = [jax.ShapeDtypeStruct(w.shape, F32) for w in ws]
    outs = pl.pallas_call(
        body, name=name, grid=(1,), in_specs=specs * 4, out_specs=specs * 3, out_shape=shapes * 3,
        compiler_params=_params(),
    )(*ws, *gs, *ms, *vs)
    return outs[:n], outs[n:2 * n], outs[2 * n:]


BIG = ("w_in", "w_o_attn", "w_pw_conv", "w_out", "w_ffn_in", "w_ffn_out")
ROW_SHARDED = ("w_out", "w_ffn_out")
SMALL = ("norm1_w", "b_gate", "q_norm_w", "k_norm_w", "conv_w", "conv_b", "conv_ln_w", "conv_ln_b", "norm2_w")
ORDER = ("norm1_w", "w_in", "b_gate", "q_norm_w", "k_norm_w", "w_o_attn", "conv_w", "conv_b", "conv_ln_w",
         "conv_ln_b", "w_pw_conv", "w_out", "norm2_w", "w_ffn_in", "w_ffn_out")
PACK_TILE = 8 * LANES


def _pack_small(parts):
    rows = []
    for p in parts:
        flat = p.reshape(-1)
        pad = (-flat.shape[0]) % PACK_TILE
        rows.append(jnp.pad(flat, (0, pad)).reshape(-1, LANES))
    return jnp.concatenate(rows, axis=0)


def _unpack_small(packed, shapes):
    out, row = [], 0
    for shp in shapes:
        size = int(np.prod(shp))
        nrow = -(-size // PACK_TILE) * (PACK_TILE // LANES)
        out.append(packed[row:row + nrow].reshape(-1)[:size].reshape(shp))
        row += nrow
    return out


def kernel(x, positions, norm1_w, w_in, b_gate, q_norm_w, k_norm_w, w_o_attn, conv_w, conv_b, conv_ln_w, conv_ln_b, w_pw_conv, w_out, norm2_w, w_ffn_in, w_ffn_out, loss_target, m_norm1_w, m_w_in, m_b_gate, m_q_norm_w, m_k_norm_w, m_w_o_attn, m_conv_w, m_conv_b, m_conv_ln_w, m_conv_ln_b, m_w_pw_conv, m_w_out, m_norm2_w, m_w_ffn_in, m_w_ffn_out, v_norm1_w, v_w_in, v_b_gate, v_q_norm_w, v_k_norm_w, v_w_o_attn, v_conv_w, v_conv_b, v_conv_ln_w, v_conv_ln_b, v_w_pw_conv, v_w_out, v_norm2_w, v_w_ffn_in, v_w_ffn_out):
    w = dict(norm1_w=norm1_w, w_in=w_in, b_gate=b_gate, q_norm_w=q_norm_w, k_norm_w=k_norm_w, w_o_attn=w_o_attn,
             conv_w=conv_w, conv_b=conv_b, conv_ln_w=conv_ln_w, conv_ln_b=conv_ln_b, w_pw_conv=w_pw_conv,
             w_out=w_out, norm2_w=norm2_w, w_ffn_in=w_ffn_in, w_ffn_out=w_ffn_out)
    m = dict(norm1_w=m_norm1_w, w_in=m_w_in, b_gate=m_b_gate, q_norm_w=m_q_norm_w, k_norm_w=m_k_norm_w,
             w_o_attn=m_w_o_attn, conv_w=m_conv_w, conv_b=m_conv_b, conv_ln_w=m_conv_ln_w,
             conv_ln_b=m_conv_ln_b, w_pw_conv=m_w_pw_conv, w_out=m_w_out, norm2_w=m_norm2_w,
             w_ffn_in=m_w_ffn_in, w_ffn_out=m_w_ffn_out)
    v = dict(norm1_w=v_norm1_w, w_in=v_w_in, b_gate=v_b_gate, q_norm_w=v_q_norm_w, k_norm_w=v_k_norm_w,
             w_o_attn=v_w_o_attn, conv_w=v_conv_w, conv_b=v_conv_b, conv_ln_w=v_conv_ln_w,
             conv_ln_b=v_conv_ln_b, w_pw_conv=v_w_pw_conv, w_out=v_w_out, norm2_w=v_norm2_w,
             w_ffn_in=v_w_ffn_in, w_ffn_out=v_w_ffn_out)
    cx, cy, cc = _mesh_pos()
    chip = 2 * cx + cy

    chip_arr = chip.reshape(1).astype(jnp.int32)
    pos_arr = jnp.stack([cc, chip]).astype(jnp.int32)
    halves = lambda buf: buf.reshape(N_CHIPS, 2, buf.shape[1] // 2, buf.shape[2])
    w_in_buf = halves(_cast_into_slot(w["w_in"][0], chip_arr, BF16, "cast_w_in"))
    small_bufs = [_cast_into_slot(w[n][0], chip_arr, F32, f"slot_{n}") for n in ("conv_w", "b_gate")]
    first_gather, started = _split_start(_gather_both_legs_rider([w_in_buf], small_bufs), "allgather_w_in_start")
    late_bufs = [halves(_cast_into_slot(w[n][0], chip_arr, BF16, f"cast_{n}", after=[started]))
                 for n in LATE_GATHER]
    wts = dict(norm1_w=norm1_w, q_norm_w=q_norm_w, k_norm_w=k_norm_w, conv_b=conv_b, conv_ln_w=conv_ln_w,
               conv_ln_b=conv_ln_b, norm2_w=norm2_w)

    loss, grad_x, g, reduced, w_in_in_flight = _forward_backward(
        x[0], positions.reshape(-1, 1), loss_target[0], wts, first_gather, late_bufs, pos_arr)
    grads = {n: b.reshape(-1, b.shape[2]) for n, b in reduced.items()}

    w_in_in_flight, started = w_in_in_flight
    delta, new_m, new_v = {}, {}, {}
    for n in EARLY_REDUCE:
        grads[n], delta[n], new_m[n], new_v[n] = _adamw(w[n][0], grads[n], m[n][0], v[n][0], f"adamw_{n}",
                                                        after=[started])
    small_parts = [loss] + [g[n] for n in SMALL]
    small_shapes = [p.shape for p in small_parts]
    device_arr = (4 * cx + 2 * cy + cc).reshape(1).astype(jnp.int32)
    small_buf = _cast_into_slot(_pack_small(small_parts), device_arr, F32, "slot_small", n_slots=N_DEVICES)

    (by_chip_w_in,), _ = _split_wait(w_in_in_flight, after=[delta[n] for n in EARLY_REDUCE] + [small_buf],
                                     name="grads_w_in_exchange_wait")
    half_w_in = _sum_chips(by_chip_w_in, pos_arr, "grads_chip_sum_w_in")
    shard_w_in, small_buf = _comm_call(
        _riders_together(_pair_gather_rider([half_w_in]), _small_gather_rider(small_buf)),
        "grads_pair_gather_w_in_small_gather")
    summed = _sum_devices(small_buf, "small_sum")
    reduced = _unpack_small(summed, small_shapes)
    loss_total = reduced[0].reshape(())
    for n, r in zip(SMALL, reduced[1:]):
        grads[n] = r
    ch_shard = conv_w.shape[2]
    grads["conv_w"] = lax.dynamic_slice_in_dim(grads["conv_w"], chip * ch_shard, ch_shard, axis=1)
    d_shard = b_gate.shape[2]
    grads["b_gate"] = lax.dynamic_slice_in_dim(grads["b_gate"], chip * d_shard, d_shard, axis=1)

    grads["w_in"], delta["w_in"], new_m["w_in"], new_v["w_in"] = _adamw(
        w["w_in"][0], shard_w_in.reshape(-1, shard_w_in.shape[2]), m["w_in"][0], v["w_in"][0], "adamw_w_in")
    flat2 = lambda a: a.reshape(-1, a.shape[-1])
    d_s, m_s, v_s = _adamw_small([flat2(w[n]) for n in SMALL], [flat2(grads[n]) for n in SMALL],
                                 [flat2(m[n]) for n in SMALL], [flat2(v[n]) for n in SMALL], "adamw_small")
    for i, n in enumerate(SMALL):
        delta[n], new_m[n], new_v[n] = d_s[i], m_s[i], v_s[i]

    shaped = lambda d, n: d[n].reshape(w[n].shape)
    return (loss_total, grad_x[None], *[shaped(grads, n) for n in ORDER], *[shaped(delta, n) for n in ORDER],
            *[shaped(new_m, n) for n in ORDER], *[shaped(new_v, n) for n in ORDER])
```

```python
import functools

import numpy as np
import jax
import jax.numpy as jnp
from jax import lax
from jax.experimental import pallas as pl
from jax.experimental.pallas import tpu as pltpu

F32 = jnp.float32
BF16 = jnp.bfloat16
MESH = pl.DeviceIdType.MESH
ANY = pl.BlockSpec(memory_space=pl.ANY)

HEAD_DIM = 64
N_SLOT_HEADS = 8
DILATIONS = (1, 4, 16)
HALF_SPAN = 64
ROPE_THETA = 500000.0
ROT_DIM = 16
CONV_WIDTH = 31
EPS = 1e-6
NEG_INF = -1e30
ADAM_LR, ADAM_B1, ADAM_B2, ADAM_EPS, ADAM_WD, ADAM_STEP = 0.001, 0.9, 0.999, 1e-08, 0.01, 10

LANES = 128
QBLK = 128
KWIN = QBLK + 2 * HALF_SPAN
VMEM_LIMIT = 48 * 1024 * 1024
N_CHIPS = 4


def _params(**kw):
    return pltpu.CompilerParams(vmem_limit_bytes=VMEM_LIMIT, **kw)


class _Rider:
    def __init__(self, operands, out_shapes, aliases, scratch, start, wait, middle=None):
        self.operands, self.out_shapes, self.aliases = list(operands), list(out_shapes), dict(aliases)
        self.scratch, self.start, self.wait = list(scratch), start, wait
        self.middle = middle


def _riders_together(a, b):
    n_in, n_out, n_sc = len(a.operands), len(a.out_shapes), len(a.scratch)
    aliases = dict(a.aliases)
    aliases.update({n_in + src: n_out + dst for src, dst in b.aliases.items()})

    def start(r_in, r_out, r_sc):
        a.start(r_in[:n_in], r_out[:n_out], r_sc[:n_sc])
        b.start(r_in[n_in:], r_out[n_out:], r_sc[n_sc:])

    def wait(r_in, r_out, r_sc):
        a.wait(r_in[:n_in], r_out[:n_out], r_sc[:n_sc])
        b.wait(r_in[n_in:], r_out[n_out:], r_sc[n_sc:])

    return _Rider(a.operands + b.operands, a.out_shapes + b.out_shapes, aliases, a.scratch + b.scratch, start, wait)


def _pallas(body, *, name, grid, in_specs, out_specs, out_shape, operands, scratch_shapes=(), aliases=None,
            rider=None, after=()):
    single = not isinstance(out_specs, (list, tuple))
    out_specs_l = [out_specs] if single else list(out_specs)
    out_shape_l = [out_shape] if single else list(out_shape)
    aliases = dict(aliases or {})

    def call(fn, all_in_specs, all_out_specs, all_out_shape, all_scratch, all_aliases, all_operands):
        return pl.pallas_call(
            fn, name=name, grid=grid, in_specs=all_in_specs, out_specs=all_out_specs, out_shape=all_out_shape,
            scratch_shapes=all_scratch, input_output_aliases=all_aliases, compiler_params=_params(),
        )(*all_operands)

    if rider is None:
        n_main = len(in_specs)

        def ordered(*refs):
            body(*refs[:n_main], *refs[n_main + len(after):])

        res = call(ordered if after else body, list(in_specs) + [ANY] * len(after), out_specs_l, out_shape_l,
                   list(scratch_shapes), aliases, list(operands) + list(after))
        return res[0] if single else res
    assert not after
    n_in, n_rin = len(in_specs), len(rider.operands)
    n_out, n_rout = len(out_specs_l), len(rider.out_shapes)
    n_sc = len(scratch_shapes)

    def wrapped(*refs):
        main_in, r_in = refs[:n_in], refs[n_in:n_in + n_rin]
        o0 = n_in + n_rin
        main_out, r_out = refs[o0:o0 + n_out], refs[o0 + n_out:o0 + n_out + n_rout]
        s0 = o0 + n_out + n_rout
        main_sc, r_sc = refs[s0:s0 + n_sc], refs[s0 + n_sc:]
        ids = [pl.program_id(d) for d in range(len(grid))]
        first = functools.reduce(jnp.logical_and, [i == 0 for i in ids])
        last = functools.reduce(jnp.logical_and, [i == n - 1 for i, n in zip(ids, grid)])

        @pl.when(first)
        def _():
            rider.start(r_in, r_out, r_sc)

        body(*main_in, *main_out, *main_sc)

        @pl.when(last)
        def _():
            rider.wait(r_in, r_out, r_sc)

    for src, dst in rider.aliases.items():
        aliases[n_in + src] = n_out + dst
    res = call(wrapped, list(in_specs) + [ANY] * n_rin, out_specs_l + [ANY] * n_rout,
               out_shape_l + rider.out_shapes, list(scratch_shapes) + rider.scratch, aliases,
               list(operands) + rider.operands)
    main = res[:n_out]
    return (main[0] if single else main), res[n_out:]


def _matmul(a, b, *, mode, tm, tn, tk, out_dtype, name, b_blocked=False,
            out_blocked=None, cols_outer=False, rider=None, after=()):
    a_shape = a.shape
    if mode == "nn":
        m_dim, k_dim = a_shape
        n_dim = b.shape[0] * b.shape[2] if b_blocked else b.shape[1]
        rows, cols, red = m_dim, n_dim, k_dim
    elif mode == "nt":
        m_dim, n_dim = a_shape
        k_dim = b.shape[1] if b_blocked else b.shape[0]
        rows, cols, red = m_dim, k_dim, n_dim
    else:
        m_dim, k_dim = a_shape
        n_dim = b.shape[1]
        rows, cols, red = k_dim, n_dim, m_dim
    assert rows % tm == 0 and cols % tn == 0 and red % tk == 0, (name, rows, cols, red)
    ni, nj, nk = rows // tm, cols // tn, red // tk

    if mode == "nn":
        a_spec = pl.BlockSpec((tm, tk), lambda i, j, k: (i, k))
        if b_blocked:
            per = b.shape[2] // tn
            b_spec = pl.BlockSpec((None, tk, tn), lambda i, j, k: (j // per, k, j % per))
        else:
            b_spec = pl.BlockSpec((tk, tn), lambda i, j, k: (k, j))
        dims = (((1,), (0,)), ((), ()))
    elif mode == "nt":
        a_spec = pl.BlockSpec((tm, tk), lambda i, j, k: (i, k))
        if b_blocked:
            per = b.shape[2] // tk
            b_spec = pl.BlockSpec((None, tn, tk), lambda i, j, k: (k // per, j, k % per))
        else:
            b_spec = pl.BlockSpec((tn, tk), lambda i, j, k: (j, k))
        dims = (((1,), (1,)), ((), ()))
    else:
        a_spec = pl.BlockSpec((tk, tm), lambda i, j, k: (k, i))
        b_spec = pl.BlockSpec((tk, tn), lambda i, j, k: (k, j))
        dims = (((0,), (0,)), ((), ()))

    if out_blocked:
        per_o = (cols // out_blocked) // tn
        out_spec = pl.BlockSpec((None, tm, tn), lambda i, j, k: (j // per_o, i, j % per_o))
        out_shape = jax.ShapeDtypeStruct((out_blocked, rows, cols // out_blocked), out_dtype)
    else:
        out_spec = pl.BlockSpec((tm, tn), lambda i, j, k: (i, j))
        out_shape = jax.ShapeDtypeStruct((rows, cols), out_dtype)

    def body(a_ref, b_ref, o_ref, *acc):
        prod = lax.dot_general(a_ref[...], b_ref[...], dims, preferred_element_type=F32)
        if nk == 1:
            o_ref[...] = prod.astype(out_dtype)
        else:
            acc_ref, = acc
            k = pl.program_id(2)

            @pl.when(k == 0)
            def _():
                acc_ref[...] = prod

            @pl.when(k > 0)
            def _():
                acc_ref[...] += prod

            @pl.when(k == nk - 1)
            def _():
                o_ref[...] = acc_ref[...].astype(out_dtype)

    scratch = [pltpu.VMEM((tm, tn), F32)] if nk > 1 else []
    grid = (ni, nj, nk)
    if cols_outer:
        swap = lambda spec: pl.BlockSpec(spec.block_shape, lambda j, i, k, f=spec.index_map: f(i, j, k))
        a_spec, b_spec, out_spec, grid = swap(a_spec), swap(b_spec), swap(out_spec), (nj, ni, nk)
    return _pallas(body, name=name, grid=grid, in_specs=[a_spec, b_spec], out_specs=out_spec,
                   out_shape=out_shape, operands=[a, b], scratch_shapes=scratch, rider=rider, after=after)


def _proj_by_slot(h, w_in, order, first, count, name, proj=None, after=()):
    s, k = h.shape
    nb = w_in.shape[2]
    tm = 512
    prev = [] if proj is None else [proj]

    def body(order_ref, h_ref, w_ref, *refs):
        refs[-1][...] = jnp.dot(h_ref[...], w_ref[...], preferred_element_type=F32)

    return pl.pallas_call(
        body, name=name,
        grid_spec=pltpu.PrefetchScalarGridSpec(
            num_scalar_prefetch=1, grid=(count, s // tm),
            in_specs=[pl.BlockSpec((tm, k), lambda j, i, order_ref: (i, 0)),
                      pl.BlockSpec((None, k, nb), lambda j, i, order_ref: (order_ref[first + j], 0, 0))]
            + [ANY] * (len(prev) + len(after)),
            out_specs=pl.BlockSpec((tm, nb), lambda j, i, order_ref: (i, order_ref[first + j]))),
        out_shape=jax.ShapeDtypeStruct((s, N_CHIPS * nb), F32),
        input_output_aliases={3: 0} if prev else {}, compiler_params=_params(),
    )(order, h, w_in, *prev, *after)


def _rmsnorm_fwd(x, w, name):
    s, d = x.shape
    tm = 256

    def body(x_ref, w_ref, o_ref):
        xv = x_ref[...]
        rstd = lax.rsqrt(jnp.mean(xv * xv, axis=-1, keepdims=True) + EPS)
        o_ref[...] = (xv * rstd * w_ref[...]).astype(BF16)

    return pl.pallas_call(
        body, name=name, grid=(s // tm,),
        in_specs=[pl.BlockSpec((tm, d), lambda i: (i, 0)), pl.BlockSpec((1, d), lambda i: (0, 0))],
        out_specs=pl.BlockSpec((tm, d), lambda i: (i, 0)),
        out_shape=jax.ShapeDtypeStruct((s, d), BF16), compiler_params=_params(),
    )(x, w)


def _rmsnorm_bwd(dh, x, w, dres, name, rider=None):
    s, d = x.shape
    tm = 256

    def body(dh_ref, x_ref, w_ref, dres_ref, dx_ref, dxb_ref, dw_ref):
        xv = x_ref[...]
        rstd = lax.rsqrt(jnp.mean(xv * xv, axis=-1, keepdims=True) + EPS)
        xhat = xv * rstd
        dhv = dh_ref[...]
        g = dhv * w_ref[...]
        dx = rstd * (g - xhat * jnp.mean(g * xhat, axis=-1, keepdims=True)) + dres_ref[...]
        dx_ref[...] = dx
        dxb_ref[...] = dx.astype(BF16)
        part = jnp.sum(dhv * xhat, axis=0, keepdims=True)

        @pl.when(pl.program_id(0) == 0)
        def _():
            dw_ref[...] = part

        @pl.when(pl.program_id(0) > 0)
        def _():
            dw_ref[...] += part

    row = pl.BlockSpec((tm, d), lambda i: (i, 0))
    vec = pl.BlockSpec((1, d), lambda i: (0, 0))
    return _pallas(
        body, name=name, grid=(s // tm,), in_specs=[row, row, vec, row], out_specs=[row, row, vec],
        out_shape=[jax.ShapeDtypeStruct((s, d), F32), jax.ShapeDtypeStruct((s, d), BF16),
                   jax.ShapeDtypeStruct((1, d), F32)],
        operands=[dh, x, w, dres], rider=rider)


def _rope_consts():
    lane = np.arange(LANES)
    in_head = lane % HEAD_DIM
    inv_freq = ROPE_THETA ** (-jnp.arange(0, ROT_DIM, 2, dtype=F32) / ROT_DIM)
    invf = jnp.where(jnp.asarray(in_head < ROT_DIM), jnp.tile(inv_freq, LANES // (ROT_DIM // 2)), 0.0)
    m_a = np.where(in_head < ROT_DIM // 2, -1.0, 0.0).astype(np.float32)
    m_b = np.where((in_head >= ROT_DIM // 2) & (in_head < ROT_DIM), 1.0, 0.0).astype(np.float32)
    block_diag = (lane[:, None] // HEAD_DIM == lane[None, :] // HEAD_DIM).astype(np.float32)
    return (invf.reshape(1, LANES).astype(F32), jnp.asarray(m_a).reshape(1, LANES),
            jnp.asarray(m_b).reshape(1, LANES), jnp.asarray(block_diag, dtype=BF16))


def _head_sums(v, bd):
    hi = v.astype(BF16)
    lo = (v - hi.astype(F32)).astype(BF16)
    return jnp.dot(hi, bd, preferred_element_type=F32) + jnp.dot(lo, bd, preferred_element_type=F32)


def _qk_fwd(proj, pos_col, qw2, kw2, consts, name, rider=None):
    s = proj.shape[0]
    width = 3 * N_SLOT_HEADS * HEAD_DIM
    tm = 128
    invf, m_a, m_b, bd = consts
    scale = HEAD_DIM ** -0.5

    def body(q_ref, k_ref, pos_ref, qw_ref, kw_ref, invf_ref, ma_ref, mb_ref, bd_ref, qo_ref, ko_ref):
        ang = pos_ref[...].astype(F32) * invf_ref[...]
        cos = jnp.cos(ang)
        sin = jnp.sin(ang)
        s_a = sin * ma_ref[...]
        s_b = sin * mb_ref[...]
        bdv = bd_ref[...]
        for src, w_ref, dst, sc in ((q_ref, qw_ref, qo_ref, scale), (k_ref, kw_ref, ko_ref, 1.0)):
            for cb in range(width // LANES):
                cols = slice(cb * LANES, (cb + 1) * LANES)
                t = src[:, cols]
                rstd = lax.rsqrt(_head_sums(t * t, bdv) * (1.0 / HEAD_DIM) + EPS)
                y = t * rstd * w_ref[...]
                r = y * cos + pltpu.roll(y, LANES - 8, axis=1) * s_a + pltpu.roll(y, 8, axis=1) * s_b
                dst[:, cols] = r * sc if sc != 1.0 else r

    vec = pl.BlockSpec((1, LANES), lambda i: (0, 0))
    return _pallas(
        body, name=name, grid=(s // tm,),
        in_specs=[pl.BlockSpec((tm, width), lambda i: (i, 0)), pl.BlockSpec((tm, width), lambda i: (i, 1)),
                  pl.BlockSpec((tm, 1), lambda i: (i, 0)), vec, vec, vec, vec, vec,
                  pl.BlockSpec((LANES, LANES), lambda i: (0, 0))],
        out_specs=[pl.BlockSpec((tm, width), lambda i: (i, 0))] * 2,
        out_shape=[jax.ShapeDtypeStruct((s, width), F32)] * 2,
        operands=[proj, proj, pos_col, qw2, kw2, invf, m_a, m_b, bd], rider=rider)


def _qk_bwd(dqn, dkn, dv, da, db, dgl, proj, pos_col, qw2, kw2, consts, name, rider=None):
    s = proj.shape[0]
    width = 3 * N_SLOT_HEADS * HEAD_DIM
    ch = da.shape[1]
    gate_w = dgl.shape[1]
    out_w = 3 * width + 2 * ch + gate_w
    assert out_w == proj.shape[1]
    tm = 128
    invf, m_a, m_b, bd = consts
    scale = HEAD_DIM ** -0.5

    def body(dq_ref, dk_ref, dv_ref, da_ref, db_ref, dgl_ref, q_ref, k_ref, pos_ref, qw_ref, kw_ref,
             invf_ref, ma_ref, mb_ref, bd_ref, out_ref, dqw_ref, dkw_ref):
        ang = pos_ref[...].astype(F32) * invf_ref[...]
        cos = jnp.cos(ang)
        sin = jnp.sin(ang)
        s_a = sin * ma_ref[...]
        s_b = sin * mb_ref[...]
        bdv = bd_ref[...]
        first = pl.program_id(0) == 0
        for src, dsrc, w_ref, col0, dw_ref, sc in ((q_ref, dq_ref, qw_ref, 0, dqw_ref, scale),
                                                   (k_ref, dk_ref, kw_ref, width, dkw_ref, 1.0)):
            dw_acc = jnp.zeros((1, LANES), F32)
            for cb in range(width // LANES):
                cols = slice(cb * LANES, (cb + 1) * LANES)
                t = src[:, cols]
                dr = dsrc[:, cols]
                if sc != 1.0:
                    dr = dr * sc
                dy = dr * cos + pltpu.roll(dr * s_a, 8, axis=1) + pltpu.roll(dr * s_b, LANES - 8, axis=1)
                rstd = lax.rsqrt(_head_sums(t * t, bdv) * (1.0 / HEAD_DIM) + EPS)
                xhat = t * rstd
                g = dy * w_ref[...]
                dt = rstd * (g - xhat * (_head_sums(g * xhat, bdv) * (1.0 / HEAD_DIM)))
                out_ref[:, col0 + cb * LANES: col0 + (cb + 1) * LANES] = dt.astype(BF16)
                dw_acc = dw_acc + jnp.sum(dy * xhat, axis=0, keepdims=True)
            dw_acc = dw_acc + pltpu.roll(dw_acc, HEAD_DIM, axis=1)

            @pl.when(first)
            def _(dw_ref=dw_ref, dw_acc=dw_acc):
                dw_ref[...] = dw_acc

            @pl.when(jnp.logical_not(first))
            def _(dw_ref=dw_ref, dw_acc=dw_acc):
                dw_ref[...] += dw_acc
        out_ref[:, 2 * width: 3 * width] = dv_ref[...].astype(BF16)
        out_ref[:, 3 * width: 3 * width + ch] = da_ref[...]
        out_ref[:, 3 * width + ch: 3 * width + 2 * ch] = db_ref[...]
        out_ref[:, 3 * width + 2 * ch: out_w] = dgl_ref[...]

    vec = pl.BlockSpec((1, LANES), lambda i: (0, 0))
    blk = lambda c: pl.BlockSpec((tm, width), lambda i: (i, c))
    cblk = pl.BlockSpec((tm, ch), lambda i: (i, 0))
    return _pallas(
        body, name=name, grid=(s // tm,),
        in_specs=[blk(0), blk(0), blk(0), cblk, cblk, pl.BlockSpec((tm, gate_w), lambda i: (i, 0)),
                  blk(0), blk(1), pl.BlockSpec((tm, 1), lambda i: (i, 0)), vec, vec, vec, vec, vec,
                  pl.BlockSpec((LANES, LANES), lambda i: (0, 0))],
        out_specs=[pl.BlockSpec((tm, out_w), lambda i: (i, 0)), vec, vec],
        out_shape=[jax.ShapeDtypeStruct((s, out_w), BF16)] + [jax.ShapeDtypeStruct((1, LANES), F32)] * 2,
        operands=[dqn, dkn, dv, da, db, dgl, proj, proj, pos_col, qw2, kw2, invf, m_a, m_b, bd],
        rider=rider)


def _row_chunks(n_rows, fn, chunk=256):
    def step(i, c):
        fn(pl.ds(pl.multiple_of(i * chunk, chunk), chunk))
        return c
    lax.fori_loop(0, n_rows // chunk, step, 0)


def _to_residue_major(dst, src, s, d, dst_off=0, cast=None):
    seq = s // d
    for r in range(d):
        v = src[...] if d == 1 else src[pl.ds(r, seq, stride=d), :]
        dst[dst_off + r * seq: dst_off + (r + 1) * seq, :] = v if cast is None else v.astype(cast)


def _from_residue_major(dst, src, s, d, src_off=0):
    seq = s // d
    for r in range(d):
        v = src[src_off + r * seq: src_off + (r + 1) * seq, :]
        if d == 1:
            dst[...] = v
        else:
            dst[pl.ds(r, seq, stride=d), :] = v


def _band_bias():
    qi = lax.broadcasted_iota(jnp.int32, (QBLK, KWIN), 0)
    kj = lax.broadcasted_iota(jnp.int32, (QBLK, KWIN), 1)
    return jnp.where(jnp.abs(kj - HALF_SPAN - qi) <= HALF_SPAN, 0.0, NEG_INF).astype(F32)


def _range_bias(base, seq):
    kj = lax.broadcasted_iota(jnp.int32, (1, KWIN), 1)
    lo = (base & -seq) - base + HALF_SPAN
    return jnp.where((kj >= lo) & (kj < lo + seq), 0.0, NEG_INF).astype(F32)


def _skewed_blocks(n_blk, produce, consume):
    produce(0, 0)
    for b in range(n_blk):
        consume(b, b % 2)
        if b + 1 < n_blk:
            produce(b + 1, (b + 1) % 2)


def _block_base(b):
    return b * QBLK if isinstance(b, int) else pl.multiple_of(b * QBLK, QBLK)


def _attn_fwd(qn, kn, proj, name, rider=None):
    s = qn.shape[0]
    n_pairs = N_SLOT_HEADS * HEAD_DIM // LANES
    v_col0 = 2 * qn.shape[1] // LANES
    nt_dims = (((1,), (1,)), ((), ()))

    def body(q_ref, k_ref, v_ref, attn_ref, lse_ref, attn_b_ref, q_rm, k_rm, v_rm, acc_rm, m_rm, l_rm,
             acc_p, m_p, l_p, m_run, l_run, acc_run, band, s_buf, m_buf):
        g = pl.program_id(1)
        zpad = jnp.zeros((HALF_SPAN, LANES), BF16)
        k_rm[0:HALF_SPAN, :] = zpad
        k_rm[s + HALF_SPAN: s + 2 * HALF_SPAN, :] = zpad
        v_rm[0:HALF_SPAN, 0:LANES] = zpad
        v_rm[s + HALF_SPAN: s + 2 * HALF_SPAN, 0:LANES] = zpad

        def ones_rows(rows):
            v_rm[pl.ds(rows.start, rows.size), LANES:2 * LANES] = jnp.ones((rows.size, LANES), BF16)

        _row_chunks(s + 2 * HALF_SPAN, ones_rows, chunk=2 * HALF_SPAN)
        band[...] = _band_bias()
        lane = lax.broadcasted_iota(jnp.int32, (QBLK, LANES), 1)
        low = lane < HEAD_DIM
        n_blk = s // QBLK

        for gi, d in enumerate(DILATIONS):
            @pl.when(g == gi)
            def _(gi=gi, d=d):
                seq = s // d
                _to_residue_major(q_rm, q_ref, s, d, cast=BF16)
                _to_residue_major(k_rm, k_ref, s, d, dst_off=HALF_SPAN, cast=BF16)
                _to_residue_major(v_rm.at[:, 0:LANES], v_ref, s, d, dst_off=HALF_SPAN, cast=BF16)

                def scores(b, slot):
                    base = _block_base(b)
                    q = q_rm[pl.ds(base, QBLK), :]
                    zero = jnp.zeros_like(q)
                    q2 = jnp.concatenate([jnp.where(low, q, zero), jnp.where(low, zero, q)], axis=0)
                    sc = lax.dot_general(q2, k_rm[pl.ds(base, KWIN), :], nt_dims, preferred_element_type=F32)
                    bias = band[...] + _range_bias(base, seq)
                    for hh in range(2):
                        rows = slice(hh * QBLK, (hh + 1) * QBLK)
                        sh = sc[rows, :] + bias
                        s_buf[slot, rows, :] = sh
                        m_buf[slot, rows, :] = jnp.broadcast_to(jnp.max(sh, axis=-1, keepdims=True), (QBLK, LANES))

                def outputs(b, slot):
                    base = _block_base(b)
                    sv = s_buf[slot]
                    mb = m_buf[slot]
                    p = jnp.exp(jnp.concatenate([sv[:, 0:LANES] - mb, sv[:, LANES:2 * LANES] - mb], axis=1))
                    pv = jnp.dot(p.astype(BF16), v_rm[pl.ds(base, KWIN), :], preferred_element_type=F32)
                    rows = pl.ds(base, QBLK)
                    acc_rm[rows, :] = jnp.where(low, pv[0:QBLK, 0:LANES], pv[QBLK:2 * QBLK, 0:LANES])
                    l_rm[rows, :] = jnp.where(low, pv[0:QBLK, LANES:2 * LANES], pv[QBLK:2 * QBLK, LANES:2 * LANES])
                    m_rm[rows, :] = jnp.where(low, mb[0:QBLK, :], mb[QBLK:2 * QBLK, :])

                _skewed_blocks(n_blk, scores, outputs)
                if d == 1:
                    src = (acc_rm, m_rm, l_rm)
                else:
                    for dst_, src_ in ((acc_p, acc_rm), (m_p, m_rm), (l_p, l_rm)):
                        _from_residue_major(dst_, src_, s, d)
                    src = (acc_p, m_p, l_p)

                def combine(rows):
                    a_g, m_g, l_g = src[0][rows, :], src[1][rows, :], src[2][rows, :]
                    if gi == 0:
                        m_new, l_new, a_new = m_g, l_g, a_g
                    else:
                        m_old = m_run[rows, :]
                        m_new = jnp.maximum(m_old, m_g)
                        w_old = jnp.exp(m_old - m_new)
                        w_g = jnp.exp(m_g - m_new)
                        l_new = l_run[rows, :] * w_old + l_g * w_g
                        a_new = acc_run[rows, :] * w_old + a_g * w_g
                    if gi == len(DILATIONS) - 1:
                        out = a_new / l_new
                        attn_ref[rows, :] = out
                        attn_b_ref[rows, :] = out.astype(BF16)
                        lse_ref[rows, :] = m_new + jnp.log(l_new)
                    else:
                        m_run[rows, :] = m_new
                        l_run[rows, :] = l_new
                        acc_run[rows, :] = a_new

                _row_chunks(s, combine)

    qk_spec = pl.BlockSpec((s, LANES), lambda hp, g: (0, g * n_pairs + hp))
    v_spec = pl.BlockSpec((s, LANES), lambda hp, g: (0, v_col0 + g * n_pairs + hp))
    o_spec = pl.BlockSpec((s, LANES), lambda hp, g: (0, hp))
    f32buf = pltpu.VMEM((s, LANES), F32)
    return _pallas(
        body, name=name, grid=(n_pairs, len(DILATIONS)), in_specs=[qk_spec, qk_spec, v_spec],
        out_specs=[o_spec, o_spec, o_spec],
        out_shape=[jax.ShapeDtypeStruct((s, n_pairs * LANES), F32)] * 2
        + [jax.ShapeDtypeStruct((s, n_pairs * LANES), BF16)],
        operands=[qn, kn, proj],
        scratch_shapes=[pltpu.VMEM((s, LANES), BF16), pltpu.VMEM((s + 2 * HALF_SPAN, LANES), BF16),
                        pltpu.VMEM((s + 2 * HALF_SPAN, 2 * LANES), BF16)] + [f32buf] * 9
        + [pltpu.VMEM((QBLK, KWIN), F32), pltpu.VMEM((2, 2 * QBLK, KWIN), F32),
           pltpu.VMEM((2, 2 * QBLK, LANES), F32)],
        rider=rider)


def _attn_bwd(qn, kn, proj, dattn, attn, lse, bd, name, rider=None):
    s = qn.shape[0]
    n_pairs = N_SLOT_HEADS * HEAD_DIM // LANES
    v_col0 = 2 * qn.shape[1] // LANES
    nt_dims = (((1,), (1,)), ((), ()))
    tn_dims = (((0,), (0,)), ((), ()))
    spad = s + 2 * HALF_SPAN

    def body(q_ref, k_ref, v_ref, do_ref, o_ref, lse_ref, bd_ref, dq_ref, dk_ref, dv_ref,
             q_rm, k_rm, v_rm, do_rm, lse0_rm, lse1_rm, dd0_rm, dd1_rm, dq_rm, dk_rm, dv_rm,
             lse0_p, lse1_p, dd0_p, dd1_p, band, p_buf, ds_buf):
        g = pl.program_id(1)
        zpad = jnp.zeros((HALF_SPAN, LANES), BF16)
        for buf in (k_rm, v_rm):
            buf[0:HALF_SPAN, :] = zpad
            buf[s + HALF_SPAN: spad, :] = zpad
        zf = jnp.zeros((HALF_SPAN, LANES), F32)
        for buf in (dk_rm, dv_rm):
            buf[0:HALF_SPAN, :] = zf
            buf[s + HALF_SPAN: spad, :] = zf
        band[...] = _band_bias()

        def clear(rows):
            z = jnp.zeros((rows.size, LANES), F32)
            dk_rm[pl.ds(rows.start + HALF_SPAN, rows.size), :] = z
            dv_rm[pl.ds(rows.start + HALF_SPAN, rows.size), :] = z

        _row_chunks(s, clear)

        def prepare(rows):
            lo = lax.broadcasted_iota(jnp.int32, (rows.size, LANES), 1) < HEAD_DIM
            dsum = _head_sums(do_ref[rows, :] * o_ref[rows, :], bd_ref[...])
            dswap = pltpu.roll(dsum, HEAD_DIM, axis=1)
            dd0_p[rows, :] = jnp.where(lo, dsum, dswap)
            dd1_p[rows, :] = jnp.where(lo, dswap, dsum)
            lv = lse_ref[rows, :]
            lswap = pltpu.roll(lv, HEAD_DIM, axis=1)
            lse0_p[rows, :] = jnp.where(lo, lv, lswap)
            lse1_p[rows, :] = jnp.where(lo, lswap, lv)

        @pl.when(g == 0)
        def _():
            _row_chunks(s, prepare)
        lane = lax.broadcasted_iota(jnp.int32, (QBLK, LANES), 1)
        low = lane < HEAD_DIM
        n_blk = s // QBLK

        def stacked(ref, rows):
            val = ref[rows, :]
            zero = jnp.zeros_like(val)
            return jnp.concatenate([jnp.where(low, val, zero), jnp.where(low, zero, val)], axis=0)

        for gi, d in enumerate(DILATIONS):
            @pl.when(g == gi)
            def _(d=d):
                seq = s // d
                _to_residue_major(q_rm, q_ref, s, d, cast=BF16)
                _to_residue_major(k_rm, k_ref, s, d, dst_off=HALF_SPAN, cast=BF16)
                _to_residue_major(v_rm, v_ref, s, d, dst_off=HALF_SPAN, cast=BF16)
                _to_residue_major(do_rm, do_ref, s, d, cast=BF16)
                for dst_, src_ in ((lse0_rm, lse0_p), (lse1_rm, lse1_p), (dd0_rm, dd0_p), (dd1_rm, dd1_p)):
                    _to_residue_major(dst_, src_, s, d)

                def scores(b, slot):
                    base = _block_base(b)
                    rows = pl.ds(base, QBLK)
                    win = pl.ds(base, KWIN)
                    sc = lax.dot_general(stacked(q_rm, rows), k_rm[win, :], nt_dims, preferred_element_type=F32)
                    dp = lax.dot_general(stacked(do_rm, rows), v_rm[win, :], nt_dims, preferred_element_type=F32)
                    bias = band[...] + _range_bias(base, seq)
                    for hh, (lse_r, dd_r) in enumerate(((lse0_rm, dd0_rm), (lse1_rm, dd1_rm))):
                        r = slice(hh * QBLK, (hh + 1) * QBLK)
                        lse_h = lse_r[rows, :]
                        dd_h = dd_r[rows, :]
                        sh = sc[r, :] + bias
                        p = jnp.exp(jnp.concatenate([sh[:, 0:LANES] - lse_h, sh[:, LANES:KWIN] - lse_h], axis=1))
                        dph = dp[r, :]
                        ds = p * jnp.concatenate([dph[:, 0:LANES] - dd_h, dph[:, LANES:KWIN] - dd_h], axis=1)
                        p_buf[slot, r, :] = p.astype(BF16)
                        ds_buf[slot, r, :] = ds.astype(BF16)

                def grads(b, slot):
                    base = _block_base(b)
                    rows = pl.ds(base, QBLK)
                    win = pl.ds(base, KWIN)
                    p = p_buf[slot]
                    ds = ds_buf[slot]
                    dq2 = jnp.dot(ds, k_rm[win, :], preferred_element_type=F32)
                    dq_rm[rows, :] = jnp.where(low, dq2[0:QBLK, :], dq2[QBLK:2 * QBLK, :])
                    dk_rm[win, :] += lax.dot_general(ds, stacked(q_rm, rows), tn_dims, preferred_element_type=F32)
                    dv_rm[win, :] += lax.dot_general(p, stacked(do_rm, rows), tn_dims, preferred_element_type=F32)

                _skewed_blocks(n_blk, scores, grads)
                _from_residue_major(dq_ref, dq_rm, s, d)
                _from_residue_major(dk_ref, dk_rm, s, d, src_off=HALF_SPAN)
                _from_residue_major(dv_ref, dv_rm, s, d, src_off=HALF_SPAN)

    qk_spec = pl.BlockSpec((s, LANES), lambda hp, g: (0, g * n_pairs + hp))
    v_spec = pl.BlockSpec((s, LANES), lambda hp, g: (0, v_col0 + g * n_pairs + hp))
    o_spec = pl.BlockSpec((s, LANES), lambda hp, g: (0, hp))
    width = qn.shape[1]
    f32buf = pltpu.VMEM((s, LANES), F32)
    f32pad = pltpu.VMEM((spad, LANES), F32)
    return _pallas(
        body, name=name, grid=(n_pairs, len(DILATIONS)),
        in_specs=[qk_spec, qk_spec, v_spec, o_spec, o_spec, o_spec,
                  pl.BlockSpec((LANES, LANES), lambda hp, g: (0, 0))],
        out_specs=[qk_spec, qk_spec, qk_spec],
        out_shape=[jax.ShapeDtypeStruct((s, width), F32)] * 3,
        operands=[qn, kn, proj, dattn, attn, lse, bd],
        scratch_shapes=[pltpu.VMEM((s, LANES), BF16), pltpu.VMEM((spad, LANES), BF16),
                        pltpu.VMEM((spad, LANES), BF16), pltpu.VMEM((s, LANES), BF16),
                        f32buf, f32buf, f32buf, f32buf, f32buf, f32pad, f32pad,
                        f32buf, f32buf, f32buf, f32buf, pltpu.VMEM((QBLK, KWIN), F32),
                        pltpu.VMEM((2, 2 * QBLK, KWIN), BF16), pltpu.VMEM((2, 2 * QBLK, KWIN), BF16)],
        rider=rider)


CONV_PAD = 16


def _conv_fwd(proj, conv_w, conv_b, col0, name, rider=None):
    s = proj.shape[0]
    ch = conv_w.shape[1]
    nblk = ch // LANES
    a0 = col0 // LANES
    tr = 256
    shift = CONV_PAD - (CONV_WIDTH - 1) // 2

    def body(a_ref, b_ref, w_ref, bias_ref, u0_ref, uc_ref, pad):
        z = jnp.zeros((CONV_PAD, LANES), F32)
        pad[0:CONV_PAD, :] = z
        pad[s + CONV_PAD: s + 2 * CONV_PAD, :] = z

        def glu(rows):
            u0 = a_ref[rows, :] * jax.nn.sigmoid(b_ref[rows, :])
            u0_ref[rows, :] = u0
            pad[pl.ds(rows.start + CONV_PAD, rows.size), :] = u0

        _row_chunks(s, glu)
        for t in range(0, s, tr):
            acc = jnp.broadcast_to(bias_ref[...], (tr, LANES))
            for k in range(CONV_WIDTH):
                acc = acc + w_ref[k:k + 1, :] * pad[t + k + shift: t + k + shift + tr, :]
            uc_ref[t:t + tr, :] = acc

    return _pallas(
        body, name=name, grid=(nblk,),
        in_specs=[pl.BlockSpec((s, LANES), lambda c: (0, a0 + c)),
                  pl.BlockSpec((s, LANES), lambda c: (0, a0 + nblk + c)),
                  pl.BlockSpec((CONV_WIDTH, LANES), lambda c: (0, c)),
                  pl.BlockSpec((1, LANES), lambda c: (0, c))],
        out_specs=[pl.BlockSpec((s, LANES), lambda c: (0, c))] * 2,
        out_shape=[jax.ShapeDtypeStruct((s, ch), F32)] * 2, operands=[proj, proj, conv_w, conv_b],
        scratch_shapes=[pltpu.VMEM((s + 2 * CONV_PAD, LANES), F32)], rider=rider)


def _ln_silu_fwd(uc, ln_w, ln_b, name):
    s, ch = uc.shape
    tm = 256

    def body(u_ref, w_ref, b_ref, o_ref):
        u = u_ref[...]
        mu = jnp.mean(u, axis=-1, keepdims=True)
        xc = u - mu
        rstd = lax.rsqrt(jnp.mean(xc * xc, axis=-1, keepdims=True) + EPS)
        z = xc * rstd * w_ref[...] + b_ref[...]
        o_ref[...] = (z * jax.nn.sigmoid(z)).astype(BF16)

    row = pl.BlockSpec((tm, ch), lambda i: (i, 0))
    vec = pl.BlockSpec((1, ch), lambda i: (0, 0))
    return pl.pallas_call(
        body, name=name, grid=(s // tm,), in_specs=[row, vec, vec], out_specs=row,
        out_shape=jax.ShapeDtypeStruct((s, ch), BF16), compiler_params=_params(),
    )(uc, ln_w, ln_b)


def _ln_silu_bwd(du3, uc, ln_w, ln_b, name):
    s, ch = uc.shape
    tm = 256

    def body(d_ref, u_ref, w_ref, b_ref, du_ref, dw_ref, db_ref):
        u = u_ref[...]
        mu = jnp.mean(u, axis=-1, keepdims=True)
        xc = u - mu
        rstd = lax.rsqrt(jnp.mean(xc * xc, axis=-1, keepdims=True) + EPS)
        xhat = xc * rstd
        z = xhat * w_ref[...] + b_ref[...]
        sg = jax.nn.sigmoid(z)
        dz = d_ref[...] * (sg * (1.0 + z * (1.0 - sg)))
        dxh = dz * w_ref[...]
        du_ref[...] = rstd * (dxh - jnp.mean(dxh, axis=-1, keepdims=True)
                              - xhat * jnp.mean(dxh * xhat, axis=-1, keepdims=True))
        pw = jnp.sum(dz * xhat, axis=0, keepdims=True)
        pb = jnp.sum(dz, axis=0, keepdims=True)
        first = pl.program_id(0) == 0

        @pl.when(first)
        def _():
            dw_ref[...] = pw
            db_ref[...] = pb

        @pl.when(jnp.logical_not(first))
        def _():
            dw_ref[...] += pw
            db_ref[...] += pb

    row = pl.BlockSpec((tm, ch), lambda i: (i, 0))
    vec = pl.BlockSpec((1, ch), lambda i: (0, 0))
    return pl.pallas_call(
        body, name=name, grid=(s // tm,), in_specs=[row, row, vec, vec], out_specs=[row, vec, vec],
        out_shape=[jax.ShapeDtypeStruct((s, ch), F32), jax.ShapeDtypeStruct((1, ch), F32),
                   jax.ShapeDtypeStruct((1, ch), F32)],
        compiler_params=_params(),
    )(du3, uc, ln_w, ln_b)


def _conv_bwd(duc, u0, proj, conv_w, col0, name, rider=None):
    s = proj.shape[0]
    ch = conv_w.shape[1]
    nblk = ch // LANES
    a0 = col0 // LANES
    tr = 256
    half = (CONV_WIDTH - 1) // 2
    shift = CONV_PAD - half

    def body(duc_ref, u0_ref, a_ref, b_ref, w_ref, da_ref, db_ref, dw_ref, dbias_ref, pad_d, pad_u):
        z = jnp.zeros((CONV_PAD, LANES), F32)
        for buf in (pad_d, pad_u):
            buf[0:CONV_PAD, :] = z
            buf[s + CONV_PAD: s + 2 * CONV_PAD, :] = z

        def fill(rows):
            dst = pl.ds(rows.start + CONV_PAD, rows.size)
            pad_d[dst, :] = duc_ref[rows, :]
            pad_u[dst, :] = u0_ref[rows, :]

        _row_chunks(s, fill)
        dw_acc = [jnp.zeros((8, LANES), F32) for _ in range(CONV_WIDTH)]
        dbias_acc = jnp.zeros((8, LANES), F32)
        for t in range(0, s, tr):
            d_t = duc_ref[t:t + tr, :]
            dbias_acc = dbias_acc + jnp.sum(d_t.reshape(tr // 8, 8, LANES), axis=0)
            du0 = jnp.zeros((tr, LANES), F32)
            for k in range(CONV_WIDTH):
                du0 = du0 + w_ref[k:k + 1, :] * pad_d[t - k + half + CONV_PAD: t - k + half + CONV_PAD + tr, :]
                prod = d_t * pad_u[t + k + shift: t + k + shift + tr, :]
                dw_acc[k] = dw_acc[k] + jnp.sum(prod.reshape(tr // 8, 8, LANES), axis=0)
            av = a_ref[t:t + tr, :]
            sg = jax.nn.sigmoid(b_ref[t:t + tr, :])
            da_ref[t:t + tr, :] = (du0 * sg).astype(BF16)
            db_ref[t:t + tr, :] = (du0 * av * sg * (1.0 - sg)).astype(BF16)
        for k in range(CONV_WIDTH):
            dw_ref[k:k + 1, :] = jnp.sum(dw_acc[k], axis=0, keepdims=True)
        dbias_ref[...] = jnp.sum(dbias_acc, axis=0, keepdims=True)

    col = lambda off: pl.BlockSpec((s, LANES), lambda c: (0, off + c))
    return _pallas(
        body, name=name, grid=(nblk,),
        in_specs=[col(0), col(0), col(a0), col(a0 + nblk),
                  pl.BlockSpec((CONV_WIDTH, LANES), lambda c: (0, c))],
        out_specs=[col(0), col(0), pl.BlockSpec((CONV_WIDTH, LANES), lambda c: (0, c)),
                   pl.BlockSpec((1, LANES), lambda c: (0, c))],
        out_shape=[jax.ShapeDtypeStruct((s, ch), BF16)] * 2
        + [jax.ShapeDtypeStruct((CONV_WIDTH, ch), F32), jax.ShapeDtypeStruct((1, ch), F32)],
        operands=[duc, u0, proj, proj, conv_w],
        scratch_shapes=[pltpu.VMEM((s + 2 * CONV_PAD, LANES), F32)] * 2, rider=rider)


GATE_BLK = 512


def _gate_fwd(proj, bg, y_a, y_b, col0, name):
    s, d = y_a.shape
    tm = 256
    g0 = col0 // GATE_BLK
    nb = d // GATE_BLK

    def body(ga_ref, gb_ref, ba_ref, bb_ref, ya_ref, yb_ref, o_ref):
        g_a = jax.nn.sigmoid(ga_ref[...] + ba_ref[...])
        g_b = jax.nn.sigmoid(gb_ref[...] + bb_ref[...])
        o_ref[...] = (g_a * ya_ref[...] + g_b * yb_ref[...]).astype(BF16)

    act = pl.BlockSpec((tm, GATE_BLK), lambda i, j: (i, j))
    return pl.pallas_call(
        body, name=name, grid=(s // tm, nb),
        in_specs=[pl.BlockSpec((tm, GATE_BLK), lambda i, j: (i, g0 + j)),
                  pl.BlockSpec((tm, GATE_BLK), lambda i, j: (i, g0 + nb + j)),
                  pl.BlockSpec((None, 1, GATE_BLK), lambda i, j: (0, 0, j)),
                  pl.BlockSpec((None, 1, GATE_BLK), lambda i, j: (1, 0, j)), act, act],
        out_specs=act, out_shape=jax.ShapeDtypeStruct((s, d), BF16), compiler_params=_params(),
    )(proj, proj, bg, bg, y_a, y_b)


def _out_proj_bwd_gates(dx1, w_out, proj, bg, y_a, y_b, col0, name, after=()):
    s, d = y_a.shape
    tm = 256
    half = d // 2
    assert col0 % half == 0
    c0 = col0 // half
    nt_dims = (((1,), (1,)), ((), ()))

    def body(dx_ref, w_ref, a0_ref, a1_ref, b0_ref, b1_ref, bias_ref, ya_ref, yb_ref,
             dgl_ref, dya_ref, dyb_ref, db_ref):
        dm = lax.dot_general(dx_ref[...], w_ref[...], nt_dims, preferred_element_type=F32)
        parts = []
        for br, (lo_ref, hi_ref, y_ref, dy_ref) in enumerate(((a0_ref, a1_ref, ya_ref, dya_ref),
                                                              (b0_ref, b1_ref, yb_ref, dyb_ref))):
            logits = jnp.concatenate([lo_ref[...], hi_ref[...]], axis=1)
            gate = jax.nn.sigmoid(logits + bias_ref[br])
            dy_ref[...] = (dm * gate).astype(BF16)
            dgl = dm * y_ref[...] * gate * (1.0 - gate)
            dgl_ref[:, br * d:(br + 1) * d] = dgl.astype(BF16)
            parts.append(jnp.sum(dgl, axis=0, keepdims=True))
        part = jnp.concatenate(parts, axis=0)
        first = pl.program_id(0) == 0

        @pl.when(first)
        def _():
            db_ref[...] = part

        @pl.when(jnp.logical_not(first))
        def _():
            db_ref[...] += part

    row = pl.BlockSpec((tm, d), lambda i: (i, 0))
    logit_blk = lambda k: pl.BlockSpec((tm, half), functools.partial(lambda i, k: (i, c0 + k), k=k))
    return _pallas(
        body, name=name, grid=(s // tm,),
        in_specs=[row, pl.BlockSpec((d, d), lambda i: (0, 0)), logit_blk(0), logit_blk(1), logit_blk(2),
                  logit_blk(3), pl.BlockSpec((2, 1, d), lambda i: (0, 0, 0)), row, row],
        out_specs=[pl.BlockSpec((tm, 2 * d), lambda i: (i, 0)), row, row, pl.BlockSpec((2, d), lambda i: (0, 0))],
        out_shape=[jax.ShapeDtypeStruct((s, 2 * d), BF16), jax.ShapeDtypeStruct((s, d), BF16),
                   jax.ShapeDtypeStruct((s, d), BF16), jax.ShapeDtypeStruct((2, d), F32)],
        operands=[dx1, w_out, proj, proj, proj, proj, bg, y_a, y_b], after=after)


def _ffn_in_swiglu(h2, w_blocked, name):
    s, k = h2.shape
    nblk, _, tn = w_blocked.shape
    ff = nblk // 2 * tn
    tm = 512

    def body(a_ref, wg_ref, wu_ref, g_ref, u_ref, act_ref):
        a = a_ref[...]
        gt = jnp.dot(a, wg_ref[...], preferred_element_type=F32)
        up = jnp.dot(a, wu_ref[...], preferred_element_type=F32)
        g_ref[...] = gt
        u_ref[...] = up
        act_ref[...] = (gt * jax.nn.sigmoid(gt) * up).astype(BF16)

    out = pl.BlockSpec((tm, tn), lambda j, i: (i, j))
    return pl.pallas_call(
        body, name=name, grid=(nblk // 2, s // tm),
        in_specs=[pl.BlockSpec((tm, k), lambda j, i: (i, 0)),
                  pl.BlockSpec((None, k, tn), lambda j, i: (j, 0, 0)),
                  pl.BlockSpec((None, k, tn), lambda j, i: (nblk // 2 + j, 0, 0))],
        out_specs=[out, out, out],
        out_shape=[jax.ShapeDtypeStruct((s, ff), F32), jax.ShapeDtypeStruct((s, ff), F32),
                   jax.ShapeDtypeStruct((s, ff), BF16)],
        compiler_params=_params(),
    )(h2, w_blocked, w_blocked)


def _ffn_out_bwd_swiglu(dy, w_ffn_out, gate, up, name, rider=None):
    s, d = dy.shape
    ff = gate.shape[1]
    tm = 256
    nt_dims = (((1,), (1,)), ((), ()))

    def body(dy_ref, w_ref, g_ref, u_ref, o_ref):
        dv = lax.dot_general(dy_ref[...], w_ref[...], nt_dims, preferred_element_type=F32)
        gt = g_ref[...]
        sg = jax.nn.sigmoid(gt)
        o_ref[:, 0:ff] = (dv * u_ref[...] * (sg * (1.0 + gt * (1.0 - sg)))).astype(BF16)
        o_ref[:, ff:2 * ff] = (dv * gt * sg).astype(BF16)

    row = pl.BlockSpec((tm, ff), lambda i: (i, 0))
    return _pallas(
        body, name=name, grid=(s // tm,),
        in_specs=[pl.BlockSpec((tm, d), lambda i: (i, 0)), pl.BlockSpec((ff, d), lambda i: (0, 0)), row, row],
        out_specs=pl.BlockSpec((tm, 2 * ff), lambda i: (i, 0)),
        out_shape=jax.ShapeDtypeStruct((s, 2 * ff), BF16), operands=[dy, w_ffn_out, gate, up], rider=rider)


def _out_proj_rmsnorm(mixed, w_out, x, norm_w, name):
    s, k = mixed.shape
    d = w_out.shape[1]
    tm = 512

    def body(a_ref, w_ref, x_ref, nw_ref, x1_ref, h2_ref):
        x1 = x_ref[...] + jnp.dot(a_ref[...], w_ref[...], preferred_element_type=F32)
        x1_ref[...] = x1
        rstd = lax.rsqrt(jnp.mean(x1 * x1, axis=-1, keepdims=True) + EPS)
        h2_ref[...] = (x1 * rstd * nw_ref[...]).astype(BF16)

    row = pl.BlockSpec((tm, d), lambda i: (i, 0))
    return pl.pallas_call(
        body, name=name, grid=(s // tm,),
        in_specs=[pl.BlockSpec((tm, k), lambda i: (i, 0)), pl.BlockSpec((k, d), lambda i: (0, 0)), row,
                  pl.BlockSpec((1, d), lambda i: (0, 0))],
        out_specs=[row, row],
        out_shape=[jax.ShapeDtypeStruct((s, d), F32), jax.ShapeDtypeStruct((s, d), BF16)],
        compiler_params=_params(),
    )(mixed, w_out, x, norm_w)


def _ffn_out_loss(act, w_ffn_out, x1, target, name):
    s, k = act.shape
    d = w_ffn_out.shape[1]
    tm = 512

    def body(a_ref, w_ref, x1_ref, t_ref, dy_ref, dyb_ref, loss_ref, acc):
        y = x1_ref[...] + jnp.dot(a_ref[...], w_ref[...], preferred_element_type=F32)
        diff = y - t_ref[...]
        dy = diff * (1.0 / d)
        dy_ref[...] = dy
        dyb_ref[...] = dy.astype(BF16)
        part = jnp.sum((diff * diff).reshape(tm // 8, 8, d), axis=0)
        i = pl.program_id(0)

        @pl.when(i == 0)
        def _():
            acc[...] = part

        @pl.when(i > 0)
        def _():
            acc[...] += part

        @pl.when(i == pl.num_programs(0) - 1)
        def _():
            loss_ref[...] = (0.5 / d) * jnp.sum(jnp.sum(acc[...], axis=1, keepdims=True), axis=0, keepdims=True)

    row = pl.BlockSpec((tm, d), lambda i: (i, 0))
    return pl.pallas_call(
        body, name=name, grid=(s // tm,),
        in_specs=[pl.BlockSpec((tm, k), lambda i: (i, 0)), pl.BlockSpec((k, d), lambda i: (0, 0)), row, row],
        out_specs=[row, row, pl.BlockSpec((1, 1), lambda i: (0, 0))],
        out_shape=[jax.ShapeDtypeStruct((s, d), F32), jax.ShapeDtypeStruct((s, d), BF16),
                   jax.ShapeDtypeStruct((1, 1), F32)],
        scratch_shapes=[pltpu.VMEM((8, d), F32)], compiler_params=_params(),
    )(act, w_ffn_out, x1, target)


LATE_GATHER = ("w_o_attn", "w_pw_conv", "w_out", "w_ffn_in", "w_ffn_out")
EARLY_REDUCE = LATE_GATHER


def _blocks_by_half(g):
    if g.ndim == 2:
        g = g.reshape(N_CHIPS, g.shape[0] // N_CHIPS, g.shape[1])
    return g.reshape(N_CHIPS, 2, g.shape[1] // 2, g.shape[2])


def _forward_backward(x, pos_col, target, wts, first_gather, late_bufs, pos_arr):
    wts = dict(wts)
    consts = _rope_consts()
    bd = consts[3]
    qw2 = jnp.tile(wts["q_norm_w"], (1, LANES // HEAD_DIM))
    kw2 = jnp.tile(wts["k_norm_w"], (1, LANES // HEAD_DIM))
    qkv_w = 3 * N_SLOT_HEADS * HEAD_DIM
    conv_col0 = 3 * qkv_w

    h = _rmsnorm_fwd(x, wts["norm1_w"], "rms1_fwd")
    slot_order = (pos_arr[1] + jnp.arange(N_CHIPS, dtype=jnp.int32)) % N_CHIPS
    w_in_own = first_gather[2][0]
    proj = _proj_by_slot(h, w_in_own.reshape(N_CHIPS, -1, w_in_own.shape[3]), slot_order, 0, 1, "mm_proj_own",
                         after=list(late_bufs))
    first_gather = _split_middle(first_gather, after=[proj], name="allgather_w_in_forward")
    (w_in_buf, conv_w_buf, b_gate_buf), _ = _split_wait(first_gather, after=[], name="allgather_w_in_wait")
    wts["w_in"] = w_in_buf.reshape(N_CHIPS, -1, w_in_buf.shape[3])
    wts["conv_w"] = conv_w_buf.transpose(1, 0, 2).reshape(CONV_WIDTH, -1)
    wts["b_gate"] = b_gate_buf.transpose(1, 0, 2).reshape(2, 1, -1)
    ch = wts["conv_w"].shape[1]
    gate_col0 = conv_col0 + 2 * ch
    gathering, started = _split_start(_gather_ici_rider(late_bufs, []), "late_gather_start", after=[wts["w_in"]])
    proj = _proj_by_slot(h, wts["w_in"], slot_order, 1, N_CHIPS - 1, "mm_proj", proj=proj, after=[started])
    qn, kn = _qk_fwd(proj, pos_col, qw2, kw2, consts, "qk_fwd")
    attn, lse, attn_b = _attn_fwd(qn, kn, proj, "attn_fwd")
    late_bufs, _ = _split_wait(gathering, after=[attn_b], name="late_gather_wait")
    (u0, uc), late_bufs = _conv_fwd(proj, wts["conv_w"], wts["conv_b"], conv_col0, "conv_fwd",
                                    rider=_gather_forward_rider(late_bufs))
    for n, buf in zip(LATE_GATHER, late_bufs):
        full = buf.reshape(N_CHIPS, -1, buf.shape[3])
        wts[n] = full.reshape(-1, full.shape[2]) if n in ROW_SHARDED else full
    y_a = _matmul(attn_b, wts["w_o_attn"], mode="nn", tm=1024, tn=256, tk=512, out_dtype=F32, name="mm_ya",
                  b_blocked=True)
    u3 = _ln_silu_fwd(uc, wts["conv_ln_w"], wts["conv_ln_b"], "ln_fwd")
    y_b = _matmul(u3, wts["w_pw_conv"], mode="nn", tm=1024, tn=256, tk=512, out_dtype=F32, name="mm_yb",
                  b_blocked=True)
    mixed = _gate_fwd(proj, wts["b_gate"], y_a, y_b, gate_col0, "gate_fwd")
    x1, h2 = _out_proj_rmsnorm(mixed, wts["w_out"], x, wts["norm2_w"], "mm_x1_rms2")
    gate, up, act = _ffn_in_swiglu(h2, wts["w_ffn_in"], "mm_gu_swiglu")
    dy, dy_b16, loss = _ffn_out_loss(act, wts["w_ffn_out"], x1, target, "mm_x2_loss")

    g = {}
    by_chip = {}

    def pair_add(n, blocks, received):
        return _add_own_half(blocks, received, pos_arr, f"grads_pair_add_{n}")

    g_ffn_out = _blocks_by_half(
        _matmul(act, dy_b16, mode="tn", tm=1408, tn=1024, tk=2048, out_dtype=F32, name="mm_dwffnout"))
    dgu, (received,) = _ffn_out_bwd_swiglu(dy_b16, wts["w_ffn_out"], gate, up, "mm_dact_swiglu_bwd",
                                           rider=_pair_exchange_rider([g_ffn_out], halved=True))
    to_send, own = pair_add("w_ffn_out", g_ffn_out, received)
    dh2, (by_chip["w_ffn_out"],) = _matmul(
        dgu, wts["w_ffn_in"], mode="nt", tm=1024, tn=1024, tk=1408, out_dtype=F32, name="mm_dh2", b_blocked=True,
        rider=_chip_exchange_rider([to_send], [own]))
    g_ffn_in = _blocks_by_half(_matmul(h2, dgu, mode="tn", tm=512, tn=1408, tk=2048, out_dtype=F32,
                                       name="mm_dwffnin", out_blocked=N_CHIPS, cols_outer=True))
    exchanging, started = _split_start(_pair_exchange_rider([g_ffn_in], halved=True), "grads_ffn_in_pair_start")
    dx1, dx1_b16, g["norm2_w"] = _rmsnorm_bwd(dh2, x1, wts["norm2_w"], dy, "rms2_bwd")
    g["w_out"] = _matmul(mixed, dx1_b16, mode="tn", tm=512, tn=1024, tk=2048, out_dtype=F32, name="mm_dwout",
                         after=[started])
    dgl, dy_a, dy_b, g["b_gate"] = _out_proj_bwd_gates(dx1_b16, wts["w_out"], proj, wts["b_gate"], y_a, y_b,
                                                       gate_col0, "mm_dmixed_gate_bwd")
    (received,), (g_ffn_in,) = _split_wait(exchanging, after=[dgl], name="grads_ffn_in_pair_wait")
    ffn_in_to_send, ffn_in_own = pair_add("w_ffn_in", g_ffn_in, received)
    dattn = _matmul(dy_a, wts["w_o_attn"], mode="nt", tm=1024, tn=512, tk=256, out_dtype=F32, name="mm_dattn",
                    b_blocked=True)
    g["w_o_attn"] = _matmul(attn_b, dy_a, mode="tn", tm=512, tn=256, tk=2048, out_dtype=F32, name="mm_dwo",
                            out_blocked=N_CHIPS)
    du3 = _matmul(dy_b, wts["w_pw_conv"], mode="nt", tm=1024, tn=512, tk=256, out_dtype=F32, name="mm_du3",
                  b_blocked=True)
    g["w_pw_conv"] = _matmul(u3, dy_b, mode="tn", tm=512, tn=256, tk=2048, out_dtype=F32, name="mm_dwpw",
                             out_blocked=N_CHIPS)
    duc, g["conv_ln_w"], g["conv_ln_b"] = _ln_silu_bwd(du3, uc, wts["conv_ln_w"], wts["conv_ln_b"], "ln_bwd")

    small3 = ("w_out", "w_o_attn", "w_pw_conv")
    g_small3 = [_blocks_by_half(g.pop(n)) for n in small3]
    (da, db, g["conv_w"], g["conv_b"]), received = _conv_bwd(
        duc, u0, proj, wts["conv_w"], conv_col0, "conv_bwd", rider=_pair_exchange_rider(g_small3, halved=True))
    sums3 = [pair_add(n, gb, rv) for n, gb, rv in zip(small3, g_small3, received)]
    (dqn, dkn, dv), (by_chip["w_ffn_in"],) = _attn_bwd(
        qn, kn, proj, dattn, attn, lse, bd, "attn_bwd",
        rider=_chip_exchange_rider([ffn_in_to_send], [ffn_in_own]))
    (dproj, dqw, dkw), exchanged3 = _qk_bwd(
        dqn, dkn, dv, da, db, dgl, proj, pos_col, qw2, kw2, consts, "qk_bwd",
        rider=_chip_exchange_rider([s[0] for s in sums3], [s[1] for s in sums3]))
    by_chip.update(zip(small3, exchanged3))
    halves = [_sum_chips(by_chip[n], pos_arr, f"grads_chip_sum_{n}") for n in EARLY_REDUCE]
    g["q_norm_w"] = dqw[:, :HEAD_DIM]
    g["k_norm_w"] = dkw[:, :HEAD_DIM]

    c = pos_arr[0]
    rh = h.shape[1] // 2
    h_sibling = lax.dynamic_slice_in_dim(h, (1 - c) * rh, rh, axis=1)
    h_own = lax.dynamic_slice_in_dim(h, c * rh, rh, axis=1)
    g_sibling, shards = _matmul(h_sibling, dproj, mode="tn", tm=rh, tn=1920, tk=2048, out_dtype=F32,
                                name="mm_dwin_sibling", out_blocked=N_CHIPS, rider=_pair_gather_rider(halves))
    reduced = dict(zip(EARLY_REDUCE, shards))
    exchanging, started = _split_start(_pair_exchange_rider([g_sibling], halved=False), "grads_w_in_pair_start")
    g_own = _matmul(h_own, dproj, mode="tn", tm=rh, tn=1920, tk=2048, out_dtype=F32, name="mm_dwin_own",
                    out_blocked=N_CHIPS, after=[started])
    (from_sibling,), _ = _split_wait(exchanging, after=[g_own], name="grads_w_in_pair_wait")
    to_send, own = _add_own_half(g_own, from_sibling, pos_arr, "grads_pair_add_w_in")
    in_flight, started = _split_start(_chip_exchange_rider([to_send], [own]), "grads_w_in_exchange_start")
    dh = _matmul(dproj, wts["w_in"], mode="nt", tm=1024, tn=1024, tk=1920, out_dtype=F32, name="mm_dh",
                 b_blocked=True, after=[started])
    grad_x, _, g["norm1_w"] = _rmsnorm_bwd(dh, x, wts["norm1_w"], dx1, "rms1_bwd")
    return loss, grad_x, g, reduced, (in_flight, started)


def _mesh_pos():
    return lax.axis_index("x"), lax.axis_index("y"), lax.axis_index("c")


def _other_chips(x, y):
    return [(1 - x, y), (x, 1 - y), (1 - x, 1 - y)]


def _cast_into_slot(shard, chip_arr, dtype, name, n_slots=N_CHIPS, after=()):
    r, c = shard.shape
    tr = r // 2 if r % 32 == 0 else r

    def body(chip_ref, s_ref, *refs):
        refs[-1][...] = s_ref[...].astype(dtype)

    return pl.pallas_call(
        body, name=name,
        grid_spec=pltpu.PrefetchScalarGridSpec(
            num_scalar_prefetch=1, grid=(r // tr,),
            in_specs=[pl.BlockSpec((tr, c), lambda i, chip_ref: (i, 0))] + [ANY] * len(after),
            out_specs=pl.BlockSpec((None, tr, c), lambda i, chip_ref: (chip_ref[0], i, 0))),
        out_shape=jax.ShapeDtypeStruct((n_slots, r, c), dtype), compiler_params=_params(),
    )(chip_arr, shard, *after)


GATHER_CHUNKS = 4


def _gather_both_legs_rider(big, small):
    nb = len(big)
    n = nb + len(small)
    nch = GATHER_CHUNKS

    def part(bufs, a, slot, half, ch):
        if a >= nb:
            return bufs[a].at[slot]
        rows = bufs[a].shape[2] // nch
        return bufs[a].at[slot, half, pl.ds(ch * rows, rows)]

    def pieces():
        return [(a, ch, k) for ch in range(nch) for a in range(n) for k in range(3) if a < nb or ch == 0]

    def ici(bufs, sems, a, ch, k, slot_of_src):
        x, y, c = _mesh_pos()
        px, py = _other_chips(x, y)[k]
        slot = 2 * x + y if slot_of_src == "mine" else 2 * px + py
        return pltpu.make_async_remote_copy(
            src_ref=part(bufs, a, slot, c, ch), dst_ref=part(bufs, a, slot, c, ch), send_sem=sems[0].at[a, ch, k],
            recv_sem=sems[1].at[a, ch, k], device_id=(px, py, c), device_id_type=MESH)

    def forward(bufs, sems, a, ch, k, half):
        x, y, c = _mesh_pos()
        px, py = _other_chips(x, y)[k]
        h = c if half == "mine" else 1 - c
        return pltpu.make_async_remote_copy(
            src_ref=part(bufs, a, 2 * px + py, h, ch), dst_ref=part(bufs, a, 2 * px + py, h, ch),
            send_sem=sems[2].at[a, ch, k], recv_sem=sems[3].at[a, ch, k], device_id=(x, y, 1 - c),
            device_id_type=MESH)

    def start(r_in, bufs, sems):
        for a, ch, k in pieces():
            ici(bufs, sems, a, ch, k, "mine").start()

    def middle(r_in, bufs, sems):
        for a, ch, k in pieces():
            ici(bufs, sems, a, ch, k, "theirs").wait_recv()
            if a < nb:
                forward(bufs, sems, a, ch, k, "mine").start()
        for a, ch, k in pieces():
            ici(bufs, sems, a, ch, k, "mine").wait_send()

    def wait(r_in, bufs, sems):
        for a, ch, k in pieces():
            if a < nb:
                forward(bufs, sems, a, ch, k, "theirs").wait_recv()
        for a, ch, k in pieces():
            if a < nb:
                forward(bufs, sems, a, ch, k, "mine").wait_send()

    ops = list(big) + list(small)
    return _Rider(ops, [jax.ShapeDtypeStruct(o.shape, o.dtype) for o in ops], {i: i for i in range(n)},
                  [pltpu.SemaphoreType.DMA((n, nch, 3)), pltpu.SemaphoreType.DMA((n, nch, 3)),
                   pltpu.SemaphoreType.DMA((nb, nch, 3)), pltpu.SemaphoreType.DMA((nb, nch, 3))],
                  start, wait, middle)


def _comm_call(rider, name):
    def body():
        pass

    return _pallas(body, name=name, grid=(1,), in_specs=[], out_specs=[], out_shape=[], operands=[],
                   rider=rider)[1]


def _gather_ici_rider(big, small):
    nb = len(big)
    n = nb + len(small)

    def copies(bufs, sems):
        x, y, c = _mesh_pos()
        me = 2 * x + y
        part = lambda a, slot: bufs[a].at[slot, c] if a < nb else bufs[a].at[slot]
        out = []
        for a in range(n):
            for k, (px, py) in enumerate(_other_chips(x, y)):
                send = functools.partial(
                    pltpu.make_async_remote_copy,
                    src_ref=part(a, me), dst_ref=part(a, me), send_sem=sems[0].at[a, k],
                    recv_sem=sems[1].at[a, k], device_id=(px, py, c), device_id_type=MESH)
                recv = functools.partial(
                    pltpu.make_async_remote_copy,
                    src_ref=part(a, 2 * px + py), dst_ref=part(a, 2 * px + py), send_sem=sems[0].at[a, k],
                    recv_sem=sems[1].at[a, k], device_id=(px, py, c), device_id_type=MESH)
                out.append((send, recv))
        return out

    def start(r_in, r_out, sems):
        for send, _ in copies(r_out, sems):
            send().start()

    def wait(r_in, r_out, sems):
        cps = copies(r_out, sems)
        for _, recv in cps:
            recv().wait_recv()
        for send, _ in cps:
            send().wait_send()

    ops = list(big) + list(small)
    return _Rider(ops, [jax.ShapeDtypeStruct(o.shape, o.dtype) for o in ops], {i: i for i in range(n)},
                  [pltpu.SemaphoreType.DMA((n, 3)), pltpu.SemaphoreType.DMA((n, 3))], start, wait)


def _gather_forward_rider(big):
    n = len(big)

    def copies(bufs, sems):
        x, y, c = _mesh_pos()
        out = []
        for a in range(n):
            for k, (px, py) in enumerate(_other_chips(x, y)):
                slot = 2 * px + py
                send = functools.partial(
                    pltpu.make_async_remote_copy,
                    src_ref=bufs[a].at[slot, c], dst_ref=bufs[a].at[slot, c], send_sem=sems[0].at[a, k],
                    recv_sem=sems[1].at[a, k], device_id=(x, y, 1 - c), device_id_type=MESH)
                recv = functools.partial(
                    pltpu.make_async_remote_copy,
                    src_ref=bufs[a].at[slot, 1 - c], dst_ref=bufs[a].at[slot, 1 - c], send_sem=sems[0].at[a, k],
                    recv_sem=sems[1].at[a, k], device_id=(x, y, 1 - c), device_id_type=MESH)
                out.append((send, recv))
        return out

    def start(r_in, r_out, sems):
        for send, _ in copies(r_out, sems):
            send().start()

    def wait(r_in, r_out, sems):
        cps = copies(r_out, sems)
        for _, recv in cps:
            recv().wait_recv()
        for send, _ in cps:
            send().wait_send()

    return _Rider(big, [jax.ShapeDtypeStruct(o.shape, o.dtype) for o in big], {i: i for i in range(n)},
                  [pltpu.SemaphoreType.DMA((n, 3)), pltpu.SemaphoreType.DMA((n, 3))], start, wait)


def _pair_exchange_rider(gs, halved):
    n = len(gs)

    def copies(r_in, r_out, sems):
        x, y, c = _mesh_pos()
        return [pltpu.make_async_remote_copy(
            src_ref=r_in[a].at[:, 1 - c] if halved else r_in[a], dst_ref=r_out[a], send_sem=sems[0].at[a],
            recv_sem=sems[1].at[a], device_id=(x, y, 1 - c), device_id_type=MESH) for a in range(n)]

    def start(r_in, r_out, sems):
        for cp in copies(r_in, r_out, sems):
            cp.start()

    def wait(r_in, r_out, sems):
        for cp in copies(r_in, r_out, sems):
            cp.wait()

    return _Rider(gs, [jax.ShapeDtypeStruct((g.shape[0],) + g.shape[-2:], g.dtype) for g in gs], {},
                  [pltpu.SemaphoreType.DMA((n,)), pltpu.SemaphoreType.DMA((n,))], start, wait)


def _chip_exchange_rider(to_send, by_chip, row_range=None):
    n = len(to_send)

    def copies(r_in, r_out, sems):
        x, y, c = _mesh_pos()
        me = 2 * x + y
        rows = (lambda ref: ref) if row_range is None else (lambda ref: ref.at[pl.ds(*row_range)])
        out = []
        for a in range(n):
            for k, (px, py) in enumerate(_other_chips(x, y)):
                send = functools.partial(
                    pltpu.make_async_remote_copy,
                    src_ref=rows(r_in[a].at[2 * px + py]), dst_ref=rows(r_out[a].at[me]),
                    send_sem=sems[0].at[a, k], recv_sem=sems[1].at[a, k], device_id=(px, py, c),
                    device_id_type=MESH)
                recv = functools.partial(
                    pltpu.make_async_remote_copy,
                    src_ref=rows(r_in[a].at[me]), dst_ref=rows(r_out[a].at[2 * px + py]),
                    send_sem=sems[0].at[a, k], recv_sem=sems[1].at[a, k], device_id=(px, py, c),
                    device_id_type=MESH)
                out.append((send, recv))
        return out

    def start(r_in, r_out, sems):
        for send, _ in copies(r_in, r_out, sems):
            send().start()

    def wait(r_in, r_out, sems):
        cps = copies(r_in, r_out, sems)
        for _, recv in cps:
            recv().wait_recv()
        for send, _ in cps:
            send().wait_send()

    return _Rider(list(to_send) + list(by_chip), [jax.ShapeDtypeStruct(b.shape, b.dtype) for b in by_chip],
                  {n + i: i for i in range(n)},
                  [pltpu.SemaphoreType.DMA((n, 3)), pltpu.SemaphoreType.DMA((n, 3))], start, wait)


HBM = pl.BlockSpec(memory_space=pltpu.HBM)
SEM = pl.BlockSpec(memory_space=pltpu.SEMAPHORE)


_IN_FLIGHT = pltpu.CompilerParams(has_side_effects=pltpu.SideEffectType.DATAFLOW_SIDE_EFFECTING)


class _FlatSems:
    def __init__(self, ref, shape):
        self.ref, self.shape = ref, shape

    @property
    def at(self):
        return self

    def __getitem__(self, idx):
        idx = idx if isinstance(idx, tuple) else (idx,)
        flat = 0
        for i, n in zip(idx, self.shape):
            flat = flat * n + i
        return self.ref.at[flat]


def _flat_sem_types(rider):
    return tuple(pltpu.SemaphoreType.DMA((int(np.prod(s.shape)),)) for s in rider.scratch)


def _as_rider_sems(rider, refs):
    return [_FlatSems(r, s.shape) for r, s in zip(refs, rider.scratch)]


def _split_start(rider, name, after=()):
    n_in, n_out, n_sem = len(rider.operands), len(rider.out_shapes), len(rider.scratch)
    n_after = len(after)
    fresh = [j for j in range(n_out) if j not in rider.aliases.values()]
    by_out = {j: i for i, j in rider.aliases.items()}

    def body(*refs):
        r_in = refs[:n_in]
        refs = refs[n_in + n_after:]
        sems = refs[:n_sem]
        thru = refs[n_sem:n_sem + n_in]
        fresh_refs = refs[n_sem + n_in:n_sem + n_in + len(fresh)]
        token = refs[-1]
        r_out = [thru[by_out[j]] if j in by_out else fresh_refs[fresh.index(j)] for j in range(n_out)]
        rider.start(r_in, r_out, _as_rider_sems(rider, sems))
        token[...] = jnp.zeros_like(token)

    res = pl.pallas_call(
        body, name=name,
        out_shape=_flat_sem_types(rider) + tuple(pltpu.HBM(o.shape, o.dtype) for o in rider.operands)
        + tuple(pltpu.HBM(rider.out_shapes[j].shape, rider.out_shapes[j].dtype) for j in fresh)
        + (jax.ShapeDtypeStruct((8, LANES), F32),),
        in_specs=(HBM,) * n_in + (ANY,) * n_after,
        out_specs=(SEM,) * n_sem + (HBM,) * (n_in + len(fresh)) + (pl.BlockSpec(memory_space=pltpu.VMEM),),
        input_output_aliases={i: n_sem + i for i in range(n_in)}, compiler_params=_IN_FLIGHT,
    )(*[pltpu.with_memory_space_constraint(o, pltpu.HBM) for o in rider.operands], *after)
    return (rider, res[:n_sem], res[n_sem:n_sem + n_in], res[n_sem + n_in:-1]), res[-1]


def _split_continue(handles, after, name, phase):
    rider, sems, thru, fresh_arrays = handles
    n_in, n_out, n_sem = len(rider.operands), len(rider.out_shapes), len(rider.scratch)
    fresh = [j for j in range(n_out) if j not in rider.aliases.values()]
    by_out = {j: i for i, j in rider.aliases.items()}
    n_data = n_in + len(fresh)

    def body(*refs):
        r_in = refs[:n_in]
        fresh_refs = refs[n_in:n_data]
        sem_refs = refs[n_data:n_data + n_sem]
        r_out = [r_in[by_out[j]] if j in by_out else fresh_refs[fresh.index(j)] for j in range(n_out)]
        phase(r_in, r_out, _as_rider_sems(rider, sem_refs))

    data = list(thru) + list(fresh_arrays)
    return pl.pallas_call(
        body, name=name, out_shape=tuple(pltpu.HBM(d.shape, d.dtype) for d in data),
        in_specs=(HBM,) * n_data + (SEM,) * n_sem + (ANY,) * len(after), out_specs=(HBM,) * n_data,
        input_output_aliases={i: i for i in range(n_data)}, compiler_params=_IN_FLIGHT,
    )(*data, *sems, *after)


def _split_middle(handles, after, name):
    rider, sems, thru, _ = handles
    res = _split_continue(handles, after, name, rider.middle)
    return rider, sems, res[:len(thru)], res[len(thru):]


def _split_wait(handles, after, name):
    rider = handles[0]
    n_in, n_out = len(rider.operands), len(rider.out_shapes)
    fresh = [j for j in range(n_out) if j not in rider.aliases.values()]
    by_out = {j: i for i, j in rider.aliases.items()}
    res = _split_continue(handles, after, name, rider.wait)
    return [res[by_out[j]] if j in by_out else res[n_in + fresh.index(j)] for j in range(n_out)], res[:n_in]


def _pair_gather_rider(bufs):
    n = len(bufs)

    def copies(r_out, sems):
        x, y, c = _mesh_pos()
        out = []
        for a in range(n):
            send = functools.partial(
                    pltpu.make_async_remote_copy,
                src_ref=r_out[a].at[c], dst_ref=r_out[a].at[c], send_sem=sems[0].at[a],
                recv_sem=sems[1].at[a], device_id=(x, y, 1 - c), device_id_type=MESH)
            recv = functools.partial(
                    pltpu.make_async_remote_copy,
                src_ref=r_out[a].at[1 - c], dst_ref=r_out[a].at[1 - c], send_sem=sems[0].at[a],
                recv_sem=sems[1].at[a], device_id=(x, y, 1 - c), device_id_type=MESH)
            out.append((send, recv))
        return out

    def start(r_in, r_out, sems):
        for send, _ in copies(r_out, sems):
            send().start()

    def wait(r_in, r_out, sems):
        cps = copies(r_out, sems)
        for _, recv in cps:
            recv().wait_recv()
        for send, _ in cps:
            send().wait_send()

    return _Rider(bufs, [jax.ShapeDtypeStruct(b.shape, b.dtype) for b in bufs], {i: i for i in range(n)},
                  [pltpu.SemaphoreType.DMA((n,)), pltpu.SemaphoreType.DMA((n,))], start, wait)


def _add_own_half(g, recv, pos_arr, name):
    nb, rh, cols = g.shape[0], g.shape[-2], g.shape[-1]

    def body(pos_ref, g_ref, r_ref, send_ref, own_ref):
        s = (g_ref[...] + r_ref[...]).astype(BF16)
        send_ref[...] = s

        @pl.when(pl.program_id(0) == pos_ref[1])
        def _():
            own_ref[...] = s

    blk = pl.BlockSpec((None, rh, cols), lambda j, pos_ref: (j, 0, 0))
    g_spec = blk if g.ndim == 3 else pl.BlockSpec((None, None, rh, cols),
                                                   lambda j, pos_ref: (j, pos_ref[0], 0, 0))
    shape = jax.ShapeDtypeStruct((nb, rh, cols), BF16)
    return pl.pallas_call(
        body, name=name,
        grid_spec=pltpu.PrefetchScalarGridSpec(
            num_scalar_prefetch=1, grid=(nb,), in_specs=[g_spec, blk],
            out_specs=[blk, pl.BlockSpec((None, rh, cols), lambda j, pos_ref: (pos_ref[1], 0, 0))]),
        out_shape=[shape, shape], compiler_params=_params(),
    )(pos_arr, g, recv)


def _sum_chips(gath, pos_arr, name):
    nb, rh, cols = gath.shape

    def body(pos_ref, a_ref, b_ref, c_ref, d_ref, o_ref):
        del pos_ref
        o_ref[...] = ((a_ref[...].astype(F32) + b_ref[...].astype(F32)) + c_ref[...].astype(F32)) \
            + d_ref[...].astype(F32)

    tr = rh // 2 if (rh // 2) % 16 == 0 else rh
    specs = [pl.BlockSpec((None, tr, cols), functools.partial(lambda i, pos_ref, j: (j, i, 0), j=j))
             for j in range(nb)]
    return pl.pallas_call(
        body, name=name,
        grid_spec=pltpu.PrefetchScalarGridSpec(
            num_scalar_prefetch=1, grid=(rh // tr,), in_specs=specs,
            out_specs=pl.BlockSpec((None, tr, cols), lambda i, pos_ref: (pos_ref[0], i, 0))),
        out_shape=jax.ShapeDtypeStruct((2, rh, cols), F32), compiler_params=_params(),
    )(pos_arr, gath, gath, gath, gath)


N_DEVICES = 8


def _small_gather_rider(buf):
    def copies(r_out, sems):
        x, y, c = _mesh_pos()
        me = 4 * x + 2 * y + c
        out = []
        for r in range(1, N_DEVICES):
            px = 1 - x if r & 4 else x
            py = 1 - y if r & 2 else y
            pc = 1 - c if r & 1 else c
            out.append(pltpu.make_async_remote_copy(
                src_ref=r_out[0].at[me], dst_ref=r_out[0].at[me], send_sem=sems[0].at[r - 1],
                recv_sem=sems[1].at[r - 1], device_id=(px, py, pc), device_id_type=MESH))
        return out

    def start(r_in, r_out, sems):
        for cp in copies(r_out, sems):
            cp.start()

    def wait(r_in, r_out, sems):
        cps = copies(r_out, sems)
        for cp in cps:
            cp.wait_recv()
        for cp in cps:
            cp.wait_send()

    return _Rider([buf], [jax.ShapeDtypeStruct(buf.shape, buf.dtype)], {0: 0},
                  [pltpu.SemaphoreType.DMA((N_DEVICES - 1,)), pltpu.SemaphoreType.DMA((N_DEVICES - 1,))],
                  start, wait)


def _sum_devices(buf, name):
    def body(b_ref, o_ref):
        acc = b_ref[0]
        for i in range(1, N_DEVICES):
            acc = acc + b_ref[i]
        o_ref[...] = acc

    return _pallas(body, name=name, grid=(1,), in_specs=[pl.BlockSpec(buf.shape, lambda i: (0, 0, 0))],
                   out_specs=pl.BlockSpec(buf.shape[1:], lambda i: (0, 0)),
                   out_shape=jax.ShapeDtypeStruct(buf.shape[1:], F32), operands=[buf])


def _adamw_math(w, g, m, v):
    m = ADAM_B1 * m + (1.0 - ADAM_B1) * g
    v = ADAM_B2 * v + (1.0 - ADAM_B2) * (g * g)
    m_hat = m / (1.0 - ADAM_B1 ** ADAM_STEP)
    v_hat = v / (1.0 - ADAM_B2 ** ADAM_STEP)
    delta = -ADAM_LR * (m_hat / (jnp.sqrt(v_hat) + ADAM_EPS) + ADAM_WD * w)
    return delta, m, v


def _adamw(w, g, m, v, name, after=()):
    r, c = w.shape
    tr = 128 if r % 128 == 0 else 64
    assert r % tr == 0

    def body(w_ref, g_ref, m_ref, v_ref, go_ref, d_ref, mo_ref, vo_ref):
        gv = g_ref[...]
        d, mn, vn = _adamw_math(w_ref[...], gv, m_ref[...], v_ref[...])
        go_ref[...] = gv
        d_ref[...] = d
        mo_ref[...] = mn
        vo_ref[...] = vn

    blk = pl.BlockSpec((tr, c), lambda i: (i, 0))
    return _pallas(body, name=name, grid=(r // tr,), in_specs=[blk] * 4, out_specs=[blk] * 4,
                   out_shape=[jax.ShapeDtypeStruct((r, c), F32)] * 4, operands=[w, g, m, v], after=after)


def _adamw_small(ws, gs, ms, vs, name):
    n = len(ws)

    def body(*refs):
        w_r, g_r, m_r, v_r = refs[:n], refs[n:2 * n], refs[2 * n:3 * n], refs[3 * n:4 * n]
        d_o, m_o, v_o = refs[4 * n:5 * n], refs[5 * n:6 * n], refs[6 * n:7 * n]
        for i in range(n):
            d, mn, vn = _adamw_math(w_r[i][...], g_r[i][...], m_r[i][...], v_r[i][...])
            d_o[i][...] = d
            m_o[i][...] = mn
            v_o[i][...] = vn

    specs = [pl.BlockSpec(w.shape, lambda i: (0, 0)) for w in ws]
    shapes = [jax.ShapeDtypeStruct(w.shape, F32) for w in ws]
    outs = pl.pallas_call(
        body, name=name, grid=(1,), in_specs=specs * 4, out_specs=specs * 3, out_shape=shapes * 3,
        compiler_params=_params(),
    )(*ws, *gs, *ms, *vs)
    return outs[:n], outs[n:2 * n], outs[2 * n:]


BIG = ("w_in", "w_o_attn", "w_pw_conv", "w_out", "w_ffn_in", "w_ffn_out")
ROW_SHARDED = ("w_out", "w_ffn_out")
SMALL = ("norm1_w", "b_gate", "q_norm_w", "k_norm_w", "conv_w", "conv_b", "conv_ln_w", "conv_ln_b", "norm2_w")
ORDER = ("norm1_w", "w_in", "b_gate", "q_norm_w", "k_norm_w", "w_o_attn", "conv_w", "conv_b", "conv_ln_w",
         "conv_ln_b", "w_pw_conv", "w_out", "norm2_w", "w_ffn_in", "w_ffn_out")
PACK_TILE = 8 * LANES


def _pack_small(parts):
    rows = []
    for p in parts:
        flat = p.reshape(-1)
        pad = (-flat.shape[0]) % PACK_TILE
        rows.append(jnp.pad(flat, (0, pad)).reshape(-1, LANES))
    return jnp.concatenate(rows, axis=0)


def _unpack_small(packed, shapes):
    out, row = [], 0
    for shp in shapes:
        size = int(np.prod(shp))
        nrow = -(-size // PACK_TILE) * (PACK_TILE // LANES)
        out.append(packed[row:row + nrow].reshape(-1)[:size].reshape(shp))
        row += nrow
    return out


def kernel(x, positions, norm1_w, w_in, b_gate, q_norm_w, k_norm_w, w_o_attn, conv_w, conv_b, conv_ln_w, conv_ln_b, w_pw_conv, w_out, norm2_w, w_ffn_in, w_ffn_out, loss_target, m_norm1_w, m_w_in, m_b_gate, m_q_norm_w, m_k_norm_w, m_w_o_attn, m_conv_w, m_conv_b, m_conv_ln_w, m_conv_ln_b, m_w_pw_conv, m_w_out, m_norm2_w, m_w_ffn_in, m_w_ffn_out, v_norm1_w, v_w_in, v_b_gate, v_q_norm_w, v_k_norm_w, v_w_o_attn, v_conv_w, v_conv_b, v_conv_ln_w, v_conv_ln_b, v_w_pw_conv, v_w_out, v_norm2_w, v_w_ffn_in, v_w_ffn_out):
    w = dict(norm1_w=norm1_w, w_in=w_in, b_gate=b_gate, q_norm_w=q_norm_w, k_norm_w=k_norm_w, w_o_attn=w_o_attn,
             conv_w=conv_w, conv_b=conv_b, conv_ln_w=conv_ln_w, conv_ln_b=conv_ln_b, w_pw_conv=w_pw_conv,
             w_out=w_out, norm2_w=norm2_w, w_ffn_in=w_ffn_in, w_ffn_out=w_ffn_out)
    m = dict(norm1_w=m_norm1_w, w_in=m_w_in, b_gate=m_b_gate, q_norm_w=m_q_norm_w, k_norm_w=m_k_norm_w,
             w_o_attn=m_w_o_attn, conv_w=m_conv_w, conv_b=m_conv_b, conv_ln_w=m_conv_ln_w,
             conv_ln_b=m_conv_ln_b, w_pw_conv=m_w_pw_conv, w_out=m_w_out, norm2_w=m_norm2_w,
             w_ffn_in=m_w_ffn_in, w_ffn_out=m_w_ffn_out)
    v = dict(norm1_w=v_norm1_w, w_in=v_w_in, b_gate=v_b_gate, q_norm_w=v_q_norm_w, k_norm_w=v_k_norm_w,
             w_o_attn=v_w_o_attn, conv_w=v_conv_w, conv_b=v_conv_b, conv_ln_w=v_conv_ln_w,
             conv_ln_b=v_conv_ln_b, w_pw_conv=v_w_pw_conv, w_out=v_w_out, norm2_w=v_norm2_w,
             w_ffn_in=v_w_ffn_in, w_ffn_out=v_w_ffn_out)
    cx, cy, cc = _mesh_pos()
    chip = 2 * cx + cy

    chip_arr = chip.reshape(1).astype(jnp.int32)
    pos_arr = jnp.stack([cc, chip]).astype(jnp.int32)
    halves = lambda buf: buf.reshape(N_CHIPS, 2, buf.shape[1] // 2, buf.shape[2])
    w_in_buf = halves(_cast_into_slot(w["w_in"][0], chip_arr, BF16, "cast_w_in"))
    small_bufs = [_cast_into_slot(w[n][0], chip_arr, F32, f"slot_{n}") for n in ("conv_w", "b_gate")]
    first_gather, started = _split_start(_gather_both_legs_rider([w_in_buf], small_bufs), "allgather_w_in_start")
    late_bufs = [halves(_cast_into_slot(w[n][0], chip_arr, BF16, f"cast_{n}", after=[started]))
                 for n in LATE_GATHER]
    wts = dict(norm1_w=norm1_w, q_norm_w=q_norm_w, k_norm_w=k_norm_w, conv_b=conv_b, conv_ln_w=conv_ln_w,
               conv_ln_b=conv_ln_b, norm2_w=norm2_w)

    loss, grad_x, g, reduced, w_in_in_flight = _forward_backward(
        x[0], positions.reshape(-1, 1), loss_target[0], wts, first_gather, late_bufs, pos_arr)
    grads = {n: b.reshape(-1, b.shape[2]) for n, b in reduced.items()}

    w_in_in_flight, started = w_in_in_flight
    delta, new_m, new_v = {}, {}, {}
    for n in EARLY_REDUCE:
        grads[n], delta[n], new_m[n], new_v[n] = _adamw(w[n][0], grads[n], m[n][0], v[n][0], f"adamw_{n}",
                                                        after=[started])
    small_parts = [loss] + [g[n] for n in SMALL]
    small_shapes = [p.shape for p in small_parts]
    device_arr = (4 * cx + 2 * cy + cc).reshape(1).astype(jnp.int32)
    small_buf = _cast_into_slot(_pack_small(small_parts), device_arr, F32, "slot_small", n_slots=N_DEVICES)

    (by_chip_w_in,), _ = _split_wait(w_in_in_flight, after=[delta[n] for n in EARLY_REDUCE] + [small_buf],
                                     name="grads_w_in_exchange_wait")
    half_w_in = _sum_chips(by_chip_w_in, pos_arr, "grads_chip_sum_w_in")
    shard_w_in, small_buf = _comm_call(
        _riders_together(_pair_gather_rider([half_w_in]), _small_gather_rider(small_buf)),
        "grads_pair_gather_w_in_small_gather")
    summed = _sum_devices(small_buf, "small_sum")
    reduced = _unpack_small(summed, small_shapes)
    loss_total = reduced[0].reshape(())
    for n, r in zip(SMALL, reduced[1:]):
        grads[n] = r
    ch_shard = conv_w.shape[2]
    grads["conv_w"] = lax.dynamic_slice_in_dim(grads["conv_w"], chip * ch_shard, ch_shard, axis=1)
    d_shard = b_gate.shape[2]
    grads["b_gate"] = lax.dynamic_slice_in_dim(grads["b_gate"], chip * d_shard, d_shard, axis=1)

    grads["w_in"], delta["w_in"], new_m["w_in"], new_v["w_in"] = _adamw(
        w["w_in"][0], shard_w_in.reshape(-1, shard_w_in.shape[2]), m["w_in"][0], v["w_in"][0], "adamw_w_in")
    flat2 = lambda a: a.reshape(-1, a.shape[-1])
    d_s, m_s, v_s = _adamw_small([flat2(w[n]) for n in SMALL], [flat2(grads[n]) for n in SMALL],
                                 [flat2(m[n]) for n in SMALL], [flat2(v[n]) for n in SMALL], "adamw_small")
    for i, n in enumerate(SMALL):
        delta[n], new_m[n], new_v[n] = d_s[i], m_s[i], v_s[i]

    shaped = lambda d, n: d[n].reshape(w[n].shape)
    return (loss_total, grad_x[None], *[shaped(grads, n) for n in ORDER], *[shaped(delta, n) for n in ORDER],
            *[shaped(new_m, n) for n in ORDER], *[shaped(new_v, n) for n in ORDER])
```

```python
import functools

import numpy as np
import jax
import jax.numpy as jnp
from jax import lax
from jax.experimental import pallas as pl
from jax.experimental.pallas import tpu as pltpu

F32 = jnp.float32
BF16 = jnp.bfloat16
MESH = pl.DeviceIdType.MESH
ANY = pl.BlockSpec(memory_space=pl.ANY)

HEAD_DIM = 64
N_SLOT_HEADS = 8
DILATIONS = (1, 4, 16)
HALF_SPAN = 64
ROPE_THETA = 500000.0
ROT_DIM = 16
CONV_WIDTH = 31
EPS = 1e-6
NEG_INF = -1e30
ADAM_LR, ADAM_B1, ADAM_B2, ADAM_EPS, ADAM_WD, ADAM_STEP = 0.001, 0.9, 0.999, 1e-08, 0.01, 10

LANES = 128
QBLK = 128
KWIN = QBLK + 2 * HALF_SPAN
VMEM_LIMIT = 48 * 1024 * 1024
N_CHIPS = 4


def _params(**kw):
    return pltpu.CompilerParams(vmem_limit_bytes=VMEM_LIMIT, **kw)


class _Rider:
    def __init__(self, operands, out_shapes, aliases, scratch, start, wait, middle=None):
        self.operands, self.out_shapes, self.aliases = list(operands), list(out_shapes), dict(aliases)
        self.scratch, self.start, self.wait = list(scratch), start, wait
        self.middle = middle


def _riders_together(a, b):
    n_in, n_out, n_sc = len(a.operands), len(a.out_shapes), len(a.scratch)
    aliases = dict(a.aliases)
    aliases.update({n_in + src: n_out + dst for src, dst in b.aliases.items()})

    def start(r_in, r_out, r_sc):
        a.start(r_in[:n_in], r_out[:n_out], r_sc[:n_sc])
        b.start(r_in[n_in:], r_out[n_out:], r_sc[n_sc:])

    def wait(r_in, r_out, r_sc):
        a.wait(r_in[:n_in], r_out[:n_out], r_sc[:n_sc])
        b.wait(r_in[n_in:], r_out[n_out:], r_sc[n_sc:])

    return _Rider(a.operands + b.operands, a.out_shapes + b.out_shapes, aliases, a.scratch + b.scratch, start, wait)


def _pallas(body, *, name, grid, in_specs, out_specs, out_shape, operands, scratch_shapes=(), aliases=None,
            rider=None, after=()):
    single = not isinstance(out_specs, (list, tuple))
    out_specs_l = [out_specs] if single else list(out_specs)
    out_shape_l = [out_shape] if single else list(out_shape)
    aliases = dict(aliases or {})

    def call(fn, all_in_specs, all_out_specs, all_out_shape, all_scratch, all_aliases, all_operands):
        return pl.pallas_call(
            fn, name=name, grid=grid, in_specs=all_in_specs, out_specs=all_out_specs, out_shape=all_out_shape,
            scratch_shapes=all_scratch, input_output_aliases=all_aliases, compiler_params=_params(),
        )(*all_operands)

    if rider is None:
        n_main = len(in_specs)

        def ordered(*refs):
            body(*refs[:n_main], *refs[n_main + len(after):])

        res = call(ordered if after else body, list(in_specs) + [ANY] * len(after), out_specs_l, out_shape_l,
                   list(scratch_shapes), aliases, list(operands) + list(after))
        return res[0] if single else res
    assert not after
    n_in, n_rin = len(in_specs), len(rider.operands)
    n_out, n_rout = len(out_specs_l), len(rider.out_shapes)
    n_sc = len(scratch_shapes)

    def wrapped(*refs):
        main_in, r_in = refs[:n_in], refs[n_in:n_in + n_rin]
        o0 = n_in + n_rin
        main_out, r_out = refs[o0:o0 + n_out], refs[o0 + n_out:o0 + n_out + n_rout]
        s0 = o0 + n_out + n_rout
        main_sc, r_sc = refs[s0:s0 + n_sc], refs[s0 + n_sc:]
        ids = [pl.program_id(d) for d in range(len(grid))]
        first = functools.reduce(jnp.logical_and, [i == 0 for i in ids])
        last = functools.reduce(jnp.logical_and, [i == n - 1 for i, n in zip(ids, grid)])

        @pl.when(first)
        def _():
            rider.start(r_in, r_out, r_sc)

        body(*main_in, *main_out, *main_sc)

        @pl.when(last)
        def _():
            rider.wait(r_in, r_out, r_sc)

    for src, dst in rider.aliases.items():
        aliases[n_in + src] = n_out + dst
    res = call(wrapped, list(in_specs) + [ANY] * n_rin, out_specs_l + [ANY] * n_rout,
               out_shape_l + rider.out_shapes, list(scratch_shapes) + rider.scratch, aliases,
               list(operands) + rider.operands)
    main = res[:n_out]
    return (main[0] if single else main), res[n_out:]


def _matmul(a, b, *, mode, tm, tn, tk, out_dtype, name, b_blocked=False,
            out_blocked=None, cols_outer=False, rider=None, after=()):
    a_shape = a.shape
    if mode == "nn":
        m_dim, k_dim = a_shape
        n_dim = b.shape[0] * b.shape[2] if b_blocked else b.shape[1]
        rows, cols, red = m_dim, n_dim, k_dim
    elif mode == "nt":
        m_dim, n_dim = a_shape
        k_dim = b.shape[1] if b_blocked else b.shape[0]
        rows, cols, red = m_dim, k_dim, n_dim
    else:
        m_dim, k_dim = a_shape
        n_dim = b.shape[1]
        rows, cols, red = k_dim, n_dim, m_dim
    assert rows % tm == 0 and cols % tn == 0 and red % tk == 0, (name, rows, cols, red)
    ni, nj, nk = rows // tm, cols // tn, red // tk

    if mode == "nn":
        a_spec = pl.BlockSpec((tm, tk), lambda i, j, k: (i, k))
        if b_blocked:
            per = b.shape[2] // tn
            b_spec = pl.BlockSpec((None, tk, tn), lambda i, j, k: (j // per, k, j % per))
        else:
            b_spec = pl.BlockSpec((tk, tn), lambda i, j, k: (k, j))
        dims = (((1,), (0,)), ((), ()))
    elif mode == "nt":
        a_spec = pl.BlockSpec((tm, tk), lambda i, j, k: (i, k))
        if b_blocked:
            per = b.shape[2] // tk
            b_spec = pl.BlockSpec((None, tn, tk), lambda i, j, k: (k // per, j, k % per))
        else:
            b_spec = pl.BlockSpec((tn, tk), lambda i, j, k: (j, k))
        dims = (((1,), (1,)), ((), ()))
    else:
        a_spec = pl.BlockSpec((tk, tm), lambda i, j, k: (k, i))
        b_spec = pl.BlockSpec((tk, tn), lambda i, j, k: (k, j))
        dims = (((0,), (0,)), ((), ()))

    if out_blocked:
        per_o = (cols // out_blocked) // tn
        out_spec = pl.BlockSpec((None, tm, tn), lambda i, j, k: (j // per_o, i, j % per_o))
        out_shape = jax.ShapeDtypeStruct((out_blocked, rows, cols // out_blocked), out_dtype)
    else:
        out_spec = pl.BlockSpec((tm, tn), lambda i, j, k: (i, j))
        out_shape = jax.ShapeDtypeStruct((rows, cols), out_dtype)

    def body(a_ref, b_ref, o_ref, *acc):
        prod = lax.dot_general(a_ref[...], b_ref[...], dims, preferred_element_type=F32)
        if nk == 1:
            o_ref[...] = prod.astype(out_dtype)
        else:
            acc_ref, = acc
            k = pl.program_id(2)

            @pl.when(k == 0)
            def _():
                acc_ref[...] = prod

            @pl.when(k > 0)
            def _():
                acc_ref[...] += prod

            @pl.when(k == nk - 1)
            def _():
                o_ref[...] = acc_ref[...].astype(out_dtype)

    scratch = [pltpu.VMEM((tm, tn), F32)] if nk > 1 else []
    grid = (ni, nj, nk)
    if cols_outer:
        swap = lambda spec: pl.BlockSpec(spec.block_shape, lambda j, i, k, f=spec.index_map: f(i, j, k))
        a_spec, b_spec, out_spec, grid = swap(a_spec), swap(b_spec), swap(out_spec), (nj, ni, nk)
    return _pallas(body, name=name, grid=grid, in_specs=[a_spec, b_spec], out_specs=out_spec,
                   out_shape=out_shape, operands=[a, b], scratch_shapes=scratch, rider=rider, after=after)


def _proj_by_slot(h, w_in, order, first, count, name, proj=None, after=()):
    s, k = h.shape
    nb = w_in.shape[2]
    tm = 512
    prev = [] if proj is None else [proj]

    def body(order_ref, h_ref, w_ref, *refs):
        refs[-1][...] = jnp.dot(h_ref[...], w_ref[...], preferred_element_type=F32)

    return pl.pallas_call(
        body, name=name,
        grid_spec=pltpu.PrefetchScalarGridSpec(
            num_scalar_prefetch=1, grid=(count, s // tm),
            in_specs=[pl.BlockSpec((tm, k), lambda j, i, order_ref: (i, 0)),
                      pl.BlockSpec((None, k, nb), lambda j, i, order_ref: (order_ref[first + j], 0, 0))]
            + [ANY] * (len(prev) + len(after)),
            out_specs=pl.BlockSpec((tm, nb), lambda j, i, order_ref: (i, order_ref[first + j]))),
        out_shape=jax.ShapeDtypeStruct((s, N_CHIPS * nb), F32),
        input_output_aliases={3: 0} if prev else {}, compiler_params=_params(),
    )(order, h, w_in, *prev, *after)


def _rmsnorm_fwd(x, w, name):
    s, d = x.shape
    tm = 256

    def body(x_ref, w_ref, o_ref):
        xv = x_ref[...]
        rstd = lax.rsqrt(jnp.mean(xv * xv, axis=-1, keepdims=True) + EPS)
        o_ref[...] = (xv * rstd * w_ref[...]).astype(BF16)

    return pl.pallas_call(
        body, name=name, grid=(s // tm,),
        in_specs=[pl.BlockSpec((tm, d), lambda i: (i, 0)), pl.BlockSpec((1, d), lambda i: (0, 0))],
        out_specs=pl.BlockSpec((tm, d), lambda i: (i, 0)),
        out_shape=jax.ShapeDtypeStruct((s, d), BF16), compiler_params=_params(),
    )(x, w)


def _rmsnorm_bwd(dh, x, w, dres, name, rider=None):
    s, d = x.shape
    tm = 256

    def body(dh_ref, x_ref, w_ref, dres_ref, dx_ref, dxb_ref, dw_ref):
        xv = x_ref[...]
        rstd = lax.rsqrt(jnp.mean(xv * xv, axis=-1, keepdims=True) + EPS)
        xhat = xv * rstd
        dhv = dh_ref[...]
        g = dhv * w_ref[...]
        dx = rstd * (g - xhat * jnp.mean(g * xhat, axis=-1, keepdims=True)) + dres_ref[...]
        dx_ref[...] = dx
        dxb_ref[...] = dx.astype(BF16)
        part = jnp.sum(dhv * xhat, axis=0, keepdims=True)

        @pl.when(pl.program_id(0) == 0)
        def _():
            dw_ref[...] = part

        @pl.when(pl.program_id(0) > 0)
        def _():
            dw_ref[...] += part

    row = pl.BlockSpec((tm, d), lambda i: (i, 0))
    vec = pl.BlockSpec((1, d), lambda i: (0, 0))
    return _pallas(
        body, name=name, grid=(s // tm,), in_specs=[row, row, vec, row], out_specs=[row, row, vec],
        out_shape=[jax.ShapeDtypeStruct((s, d), F32), jax.ShapeDtypeStruct((s, d), BF16),
                   jax.ShapeDtypeStruct((1, d), F32)],
        operands=[dh, x, w, dres], rider=rider)


def _rope_consts():
    lane = np.arange(LANES)
    in_head = lane % HEAD_DIM
    inv_freq = ROPE_THETA ** (-jnp.arange(0, ROT_DIM, 2, dtype=F32) / ROT_DIM)
    invf = jnp.where(jnp.asarray(in_head < ROT_DIM), jnp.tile(inv_freq, LANES // (ROT_DIM // 2)), 0.0)
    m_a = np.where(in_head < ROT_DIM // 2, -1.0, 0.0).astype(np.float32)
    m_b = np.where((in_head >= ROT_DIM // 2) & (in_head < ROT_DIM), 1.0, 0.0).astype(np.float32)
    block_diag = (lane[:, None] // HEAD_DIM == lane[None, :] // HEAD_DIM).astype(np.float32)
    return (invf.reshape(1, LANES).astype(F32), jnp.asarray(m_a).reshape(1, LANES),
            jnp.asarray(m_b).reshape(1, LANES), jnp.asarray(block_diag, dtype=BF16))


def _head_sums(v, bd):
    hi = v.astype(BF16)
    lo = (v - hi.astype(F32)).astype(BF16)
    return jnp.dot(hi, bd, preferred_element_type=F32) + jnp.dot(lo, bd, preferred_element_type=F32)


def _qk_fwd(proj, pos_col, qw2, kw2, consts, name, rider=None):
    s = proj.shape[0]
    width = 3 * N_SLOT_HEADS * HEAD_DIM
    tm = 128
    invf, m_a, m_b, bd = consts
    scale = HEAD_DIM ** -0.5

    def body(q_ref, k_ref, pos_ref, qw_ref, kw_ref, invf_ref, ma_ref, mb_ref, bd_ref, qo_ref, ko_ref):
        ang = pos_ref[...].astype(F32) * invf_ref[...]
        cos = jnp.cos(ang)
        sin = jnp.sin(ang)
        s_a = sin * ma_ref[...]
        s_b = sin * mb_ref[...]
        bdv = bd_ref[...]
        for src, w_ref, dst, sc in ((q_ref, qw_ref, qo_ref, scale), (k_ref, kw_ref, ko_ref, 1.0)):
            for cb in range(width // LANES):
                cols = slice(cb * LANES, (cb + 1) * LANES)
                t = src[:, cols]
                rstd = lax.rsqrt(_head_sums(t * t, bdv) * (1.0 / HEAD_DIM) + EPS)
                y = t * rstd * w_ref[...]
                r = y * cos + pltpu.roll(y, LANES - 8, axis=1) * s_a + pltpu.roll(y, 8, axis=1) * s_b
                dst[:, cols] = r * sc if sc != 1.0 else r

    vec = pl.BlockSpec((1, LANES), lambda i: (0, 0))
    return _pallas(
        body, name=name, grid=(s // tm,),
        in_specs=[pl.BlockSpec((tm, width), lambda i: (i, 0)), pl.BlockSpec((tm, width), lambda i: (i, 1)),
                  pl.BlockSpec((tm, 1), lambda i: (i, 0)), vec, vec, vec, vec, vec,
                  pl.BlockSpec((LANES, LANES), lambda i: (0, 0))],
        out_specs=[pl.BlockSpec((tm, width), lambda i: (i, 0))] * 2,
        out_shape=[jax.ShapeDtypeStruct((s, width), F32)] * 2,
        operands=[proj, proj, pos_col, qw2, kw2, invf, m_a, m_b, bd], rider=rider)


def _qk_bwd(dqn, dkn, dv, da, db, dgl, proj, pos_col, qw2, kw2, consts, name, rider=None):
    s = proj.shape[0]
    width = 3 * N_SLOT_HEADS * HEAD_DIM
    ch = da.shape[1]
    gate_w = dgl.shape[1]
    out_w = 3 * width + 2 * ch + gate_w
    assert out_w == proj.shape[1]
    tm = 128
    invf, m_a, m_b, bd = consts
    scale = HEAD_DIM ** -0.5

    def body(dq_ref, dk_ref, dv_ref, da_ref, db_ref, dgl_ref, q_ref, k_ref, pos_ref, qw_ref, kw_ref,
             invf_ref, ma_ref, mb_ref, bd_ref, out_ref, dqw_ref, dkw_ref):
        ang = pos_ref[...].astype(F32) * invf_ref[...]
        cos = jnp.cos(ang)
        sin = jnp.sin(ang)
        s_a = sin * ma_ref[...]
        s_b = sin * mb_ref[...]
        bdv = bd_ref[...]
        first = pl.program_id(0) == 0
        for src, dsrc, w_ref, col0, dw_ref, sc in ((q_ref, dq_ref, qw_ref, 0, dqw_ref, scale),
                                                   (k_ref, dk_ref, kw_ref, width, dkw_ref, 1.0)):
            dw_acc = jnp.zeros((1, LANES), F32)
            for cb in range(width // LANES):
                cols = slice(cb * LANES, (cb + 1) * LANES)
                t = src[:, cols]
                dr = dsrc[:, cols]
                if sc != 1.0:
                    dr = dr * sc
                dy = dr * cos + pltpu.roll(dr * s_a, 8, axis=1) + pltpu.roll(dr * s_b, LANES - 8, axis=1)
                rstd = lax.rsqrt(_head_sums(t * t, bdv) * (1.0 / HEAD_DIM) + EPS)
                xhat = t * rstd
                g = dy * w_ref[...]
                dt = rstd * (g - xhat * (_head_sums(g * xhat, bdv) * (1.0 / HEAD_DIM)))
                out_ref[:, col0 + cb * LANES: col0 + (cb + 1) * LANES] = dt.astype(BF16)
                dw_acc = dw_acc + jnp.sum(dy * xhat, axis=0, keepdims=True)
            dw_acc = dw_acc + pltpu.roll(dw_acc, HEAD_DIM, axis=1)

            @pl.when(first)
            def _(dw_ref=dw_ref, dw_acc=dw_acc):
                dw_ref[...] = dw_acc

            @pl.when(jnp.logical_not(first))
            def _(dw_ref=dw_ref, dw_acc=dw_acc):
                dw_ref[...] += dw_acc
        out_ref[:, 2 * width: 3 * width] = dv_ref[...].astype(BF16)
        out_ref[:, 3 * width: 3 * width + ch] = da_ref[...]
        out_ref[:, 3 * width + ch: 3 * width + 2 * ch] = db_ref[...]
        out_ref[:, 3 * width + 2 * ch: out_w] = dgl_ref[...]

    vec = pl.BlockSpec((1, LANES), lambda i: (0, 0))
    blk = lambda c: pl.BlockSpec((tm, width), lambda i: (i, c))
    cblk = pl.BlockSpec((tm, ch), lambda i: (i, 0))
    return _pallas(
        body, name=name, grid=(s // tm,),
        in_specs=[blk(0), blk(0), blk(0), cblk, cblk, pl.BlockSpec((tm, gate_w), lambda i: (i, 0)),
                  blk(0), blk(1), pl.BlockSpec((tm, 1), lambda i: (i, 0)), vec, vec, vec, vec, vec,
                  pl.BlockSpec((LANES, LANES), lambda i: (0, 0))],
        out_specs=[pl.BlockSpec((tm, out_w), lambda i: (i, 0)), vec, vec],
        out_shape=[jax.ShapeDtypeStruct((s, out_w), BF16)] + [jax.ShapeDtypeStruct((1, LANES), F32)] * 2,
        operands=[dqn, dkn, dv, da, db, dgl, proj, proj, pos_col, qw2, kw2, invf, m_a, m_b, bd],
        rider=rider)


def _row_chunks(n_rows, fn, chunk=256):
    def step(i, c):
        fn(pl.ds(pl.multiple_of(i * chunk, chunk), chunk))
        return c
    lax.fori_loop(0, n_rows // chunk, step, 0)


def _to_residue_major(dst, src, s, d, dst_off=0, cast=None):
    seq = s // d
    for r in range(d):
        v = src[...] if d == 1 else src[pl.ds(r, seq, stride=d), :]
        dst[dst_off + r * seq: dst_off + (r + 1) * seq, :] = v if cast is None else v.astype(cast)


def _from_residue_major(dst, src, s, d, src_off=0):
    seq = s // d
    for r in range(d):
        v = src[src_off + r * seq: src_off + (r + 1) * seq, :]
        if d == 1:
            dst[...] = v
        else:
            dst[pl.ds(r, seq, stride=d), :] = v


def _band_bias():
    qi = lax.broadcasted_iota(jnp.int32, (QBLK, KWIN), 0)
    kj = lax.broadcasted_iota(jnp.int32, (QBLK, KWIN), 1)
    return jnp.where(jnp.abs(kj - HALF_SPAN - qi) <= HALF_SPAN, 0.0, NEG_INF).astype(F32)


def _range_bias(base, seq):
    kj = lax.broadcasted_iota(jnp.int32, (1, KWIN), 1)
    lo = (base & -seq) - base + HALF_SPAN
    return jnp.where((kj >= lo) & (kj < lo + seq), 0.0, NEG_INF).astype(F32)


def _skewed_blocks(n_blk, produce, consume):
    produce(0, 0)
    for b in range(n_blk):
        consume(b, b % 2)
        if b + 1 < n_blk:
            produce(b + 1, (b + 1) % 2)


def _block_base(b):
    return b * QBLK if isinstance(b, int) else pl.multiple_of(b * QBLK, QBLK)


def _attn_fwd(qn, kn, proj, name, rider=None):
    s = qn.shape[0]
    n_pairs = N_SLOT_HEADS * HEAD_DIM // LANES
    v_col0 = 2 * qn.shape[1] // LANES
    nt_dims = (((1,), (1,)), ((), ()))

    def body(q_ref, k_ref, v_ref, attn_ref, lse_ref, attn_b_ref, q_rm, k_rm, v_rm, acc_rm, m_rm, l_rm,
             acc_p, m_p, l_p, m_run, l_run, acc_run, band, s_buf, m_buf):
        g = pl.program_id(1)
        zpad = jnp.zeros((HALF_SPAN, LANES), BF16)
        k_rm[0:HALF_SPAN, :] = zpad
        k_rm[s + HALF_SPAN: s + 2 * HALF_SPAN, :] = zpad
        v_rm[0:HALF_SPAN, 0:LANES] = zpad
        v_rm[s + HALF_SPAN: s + 2 * HALF_SPAN, 0:LANES] = zpad

        def ones_rows(rows):
            v_rm[pl.ds(rows.start, rows.size), LANES:2 * LANES] = jnp.ones((rows.size, LANES), BF16)

        _row_chunks(s + 2 * HALF_SPAN, ones_rows, chunk=2 * HALF_SPAN)
        band[...] = _band_bias()
        lane = lax.broadcasted_iota(jnp.int32, (QBLK, LANES), 1)
        low = lane < HEAD_DIM
        n_blk = s // QBLK

        for gi, d in enumerate(DILATIONS):
            @pl.when(g == gi)
            def _(gi=gi, d=d):
                seq = s // d
                _to_residue_major(q_rm, q_ref, s, d, cast=BF16)
                _to_residue_major(k_rm, k_ref, s, d, dst_off=HALF_SPAN, cast=BF16)
                _to_residue_major(v_rm.at[:, 0:LANES], v_ref, s, d, dst_off=HALF_SPAN, cast=BF16)

                def scores(b, slot):
                    base = _block_base(b)
                    q = q_rm[pl.ds(base, QBLK), :]
                    zero = jnp.zeros_like(q)
                    q2 = jnp.concatenate([jnp.where(low, q, zero), jnp.where(low, zero, q)], axis=0)
                    sc = lax.dot_general(q2, k_rm[pl.ds(base, KWIN), :], nt_dims, preferred_element_type=F32)
                    bias = band[...] + _range_bias(base, seq)
                    for hh in range(2):
                        rows = slice(hh * QBLK, (hh + 1) * QBLK)
                        sh = sc[rows, :] + bias
                        s_buf[slot, rows, :] = sh
                        m_buf[slot, rows, :] = jnp.broadcast_to(jnp.max(sh, axis=-1, keepdims=True), (QBLK, LANES))

                def outputs(b, slot):
                    base = _block_base(b)
                    sv = s_buf[slot]
                    mb = m_buf[slot]
                    p = jnp.exp(jnp.concatenate([sv[:, 0:LANES] - mb, sv[:, LANES:2 * LANES] - mb], axis=1))
                    pv = jnp.dot(p.astype(BF16), v_rm[pl.ds(base, KWIN), :], preferred_element_type=F32)
                    rows = pl.ds(base, QBLK)
                    acc_rm[rows, :] = jnp.where(low, pv[0:QBLK, 0:LANES], pv[QBLK:2 * QBLK, 0:LANES])
                    l_rm[rows, :] = jnp.where(low, pv[0:QBLK, LANES:2 * LANES], pv[QBLK:2 * QBLK, LANES:2 * LANES])
                    m_rm[rows, :] = jnp.where(low, mb[0:QBLK, :], mb[QBLK:2 * QBLK, :])

                _skewed_blocks(n_blk, scores, outputs)
                if d == 1:
                    src = (acc_rm, m_rm, l_rm)
                else:
                    for dst_, src_ in ((acc_p, acc_rm), (m_p, m_rm), (l_p, l_rm)):
                        _from_residue_major(dst_, src_, s, d)
                    src = (acc_p, m_p, l_p)

                def combine(rows):
                    a_g, m_g, l_g = src[0][rows, :], src[1][rows, :], src[2][rows, :]
                    if gi == 0:
                        m_new, l_new, a_new = m_g, l_g, a_g
                    else:
                        m_old = m_run[rows, :]
                        m_new = jnp.maximum(m_old, m_g)
                        w_old = jnp.exp(m_old - m_new)
                        w_g = jnp.exp(m_g - m_new)
                        l_new = l_run[rows, :] * w_old + l_g * w_g
                        a_new = acc_run[rows, :] * w_old + a_g * w_g
                    if gi == len(DILATIONS) - 1:
                        out = a_new / l_new
                        attn_ref[rows, :] = out
                        attn_b_ref[rows, :] = out.astype(BF16)
                        lse_ref[rows, :] = m_new + jnp.log(l_new)
                    else:
                        m_run[rows, :] = m_new
                        l_run[rows, :] = l_new
                        acc_run[rows, :] = a_new

                _row_chunks(s, combine)

    qk_spec = pl.BlockSpec((s, LANES), lambda hp, g: (0, g * n_pairs + hp))
    v_spec = pl.BlockSpec((s, LANES), lambda hp, g: (0, v_col0 + g * n_pairs + hp))
    o_spec = pl.BlockSpec((s, LANES), lambda hp, g: (0, hp))
    f32buf = pltpu.VMEM((s, LANES), F32)
    return _pallas(
        body, name=name, grid=(n_pairs, len(DILATIONS)), in_specs=[qk_spec, qk_spec, v_spec],
        out_specs=[o_spec, o_spec, o_spec],
        out_shape=[jax.ShapeDtypeStruct((s, n_pairs * LANES), F32)] * 2
        + [jax.ShapeDtypeStruct((s, n_pairs * LANES), BF16)],
        operands=[qn, kn, proj],
        scratch_shapes=[pltpu.VMEM((s, LANES), BF16), pltpu.VMEM((s + 2 * HALF_SPAN, LANES), BF16),
                        pltpu.VMEM((s + 2 * HALF_SPAN, 2 * LANES), BF16)] + [f32buf] * 9
        + [pltpu.VMEM((QBLK, KWIN), F32), pltpu.VMEM((2, 2 * QBLK, KWIN), F32),
           pltpu.VMEM((2, 2 * QBLK, LANES), F32)],
        rider=rider)


def _attn_bwd(qn, kn, proj, dattn, attn, lse, bd, name, rider=None):
    s = qn.shape[0]
    n_pairs = N_SLOT_HEADS * HEAD_DIM // LANES
    v_col0 = 2 * qn.shape[1] // LANES
    nt_dims = (((1,), (1,)), ((), ()))
    tn_dims = (((0,), (0,)), ((), ()))
    spad = s + 2 * HALF_SPAN

    def body(q_ref, k_ref, v_ref, do_ref, o_ref, lse_ref, bd_ref, dq_ref, dk_ref, dv_ref,
             q_rm, k_rm, v_rm, do_rm, lse0_rm, lse1_rm, dd0_rm, dd1_rm, dq_rm, dk_rm, dv_rm,
             lse0_p, lse1_p, dd0_p, dd1_p, band, p_buf, ds_buf):
        g = pl.program_id(1)
        zpad = jnp.zeros((HALF_SPAN, LANES), BF16)
        for buf in (k_rm, v_rm):
            buf[0:HALF_SPAN, :] = zpad
            buf[s + HALF_SPAN: spad, :] = zpad
        zf = jnp.zeros((HALF_SPAN, LANES), F32)
        for buf in (dk_rm, dv_rm):
            buf[0:HALF_SPAN, :] = zf
            buf[s + HALF_SPAN: spad, :] = zf
        band[...] = _band_bias()

        def clear(rows):
            z = jnp.zeros((rows.size, LANES), F32)
            dk_rm[pl.ds(rows.start + HALF_SPAN, rows.size), :] = z
            dv_rm[pl.ds(rows.start + HALF_SPAN, rows.size), :] = z

        _row_chunks(s, clear)

        def prepare(rows):
            lo = lax.broadcasted_iota(jnp.int32, (rows.size, LANES), 1) < HEAD_DIM
            dsum = _head_sums(do_ref[rows, :] * o_ref[rows, :], bd_ref[...])
            dswap = pltpu.roll(dsum, HEAD_DIM, axis=1)
            dd0_p[rows, :] = jnp.where(lo, dsum, dswap)
            dd1_p[rows, :] = jnp.where(lo, dswap, dsum)
            lv = lse_ref[rows, :]
            lswap = pltpu.roll(lv, HEAD_DIM, axis=1)
            lse0_p[rows, :] = jnp.where(lo, lv, lswap)
            lse1_p[rows, :] = jnp.where(lo, lswap, lv)

        @pl.when(g == 0)
        def _():
            _row_chunks(s, prepare)
        lane = lax.broadcasted_iota(jnp.int32, (QBLK, LANES), 1)
        low = lane < HEAD_DIM
        n_blk = s // QBLK

        def stacked(ref, rows):
            val = ref[rows, :]
            zero = jnp.zeros_like(val)
            return jnp.concatenate([jnp.where(low, val, zero), jnp.where(low, zero, val)], axis=0)

        for gi, d in enumerate(DILATIONS):
            @pl.when(g == gi)
            def _(d=d):
                seq = s // d
                _to_residue_major(q_rm, q_ref, s, d, cast=BF16)
                _to_residue_major(k_rm, k_ref, s, d, dst_off=HALF_SPAN, cast=BF16)
                _to_residue_major(v_rm, v_ref, s, d, dst_off=HALF_SPAN, cast=BF16)
                _to_residue_major(do_rm, do_ref, s, d, cast=BF16)
                for dst_, src_ in ((lse0_rm, lse0_p), (lse1_rm, lse1_p), (dd0_rm, dd0_p), (dd1_rm, dd1_p)):
                    _to_residue_major(dst_, src_, s, d)

                def scores(b, slot):
                    base = _block_base(b)
                    rows = pl.ds(base, QBLK)
                    win = pl.ds(base, KWIN)
                    sc = lax.dot_general(stacked(q_rm, rows), k_rm[win, :], nt_dims, preferred_element_type=F32)
                    dp = lax.dot_general(stacked(do_rm, rows), v_rm[win, :], nt_dims, preferred_element_type=F32)
                    bias = band[...] + _range_bias(base, seq)
                    for hh, (lse_r, dd_r) in enumerate(((lse0_rm, dd0_rm), (lse1_rm, dd1_rm))):
                        r = slice(hh * QBLK, (hh + 1) * QBLK)
                        lse_h = lse_r[rows, :]
                        dd_h = dd_r[rows, :]
                        sh = sc[r, :] + bias
                        p = jnp.exp(jnp.concatenate([sh[:, 0:LANES] - lse_h, sh[:, LANES:KWIN] - lse_h], axis=1))
                        dph = dp[r, :]
                        ds = p * jnp.concatenate([dph[:, 0:LANES] - dd_h, dph[:, LANES:KWIN] - dd_h], axis=1)
                        p_buf[slot, r, :] = p.astype(BF16)
                        ds_buf[slot, r, :] = ds.astype(BF16)

                def grads(b, slot):
                    base = _block_base(b)
                    rows = pl.ds(base, QBLK)
                    win = pl.ds(base, KWIN)
                    p = p_buf[slot]
                    ds = ds_buf[slot]
                    dq2 = jnp.dot(ds, k_rm[win, :], preferred_element_type=F32)
                    dq_rm[rows, :] = jnp.where(low, dq2[0:QBLK, :], dq2[QBLK:2 * QBLK, :])
                    dk_rm[win, :] += lax.dot_general(ds, stacked(q_rm, rows), tn_dims, preferred_element_type=F32)
                    dv_rm[win, :] += lax.dot_general(p, stacked(do_rm, rows), tn_dims, preferred_element_type=F32)

                _skewed_blocks(n_blk, scores, grads)
                _from_residue_major(dq_ref, dq_rm, s, d)
                _from_residue_major(dk_ref, dk_rm, s, d, src_off=HALF_SPAN)
                _from_residue_major(dv_ref, dv_rm, s, d, src_off=HALF_SPAN)

    qk_spec = pl.BlockSpec((s, LANES), lambda hp, g: (0, g * n_pairs + hp))
    v_spec = pl.BlockSpec((s, LANES), lambda hp, g: (0, v_col0 + g * n_pairs + hp))
    o_spec = pl.BlockSpec((s, LANES), lambda hp, g: (0, hp))
    width = qn.shape[1]
    f32buf = pltpu.VMEM((s, LANES), F32)
    f32pad = pltpu.VMEM((spad, LANES), F32)
    return _pallas(
        body, name=name, grid=(n_pairs, len(DILATIONS)),
        in_specs=[qk_spec, qk_spec, v_spec, o_spec, o_spec, o_spec,
                  pl.BlockSpec((LANES, LANES), lambda hp, g: (0, 0))],
        out_specs=[qk_spec, qk_spec, qk_spec],
        out_shape=[jax.ShapeDtypeStruct((s, width), F32)] * 3,
        operands=[qn, kn, proj, dattn, attn, lse, bd],
        scratch_shapes=[pltpu.VMEM((s, LANES), BF16), pltpu.VMEM((spad, LANES), BF16),
                        pltpu.VMEM((spad, LANES), BF16), pltpu.VMEM((s, LANES), BF16),
                        f32buf, f32buf, f32buf, f32buf, f32buf, f32pad, f32pad,
                        f32buf, f32buf, f32buf, f32buf, pltpu.VMEM((QBLK, KWIN), F32),
                        pltpu.VMEM((2, 2 * QBLK, KWIN), BF16), pltpu.VMEM((2, 2 * QBLK, KWIN), BF16)],
        rider=rider)


CONV_PAD = 16


def _conv_fwd(proj, conv_w, conv_b, col0, name, rider=None):
    s = proj.shape[0]
    ch = conv_w.shape[1]
    nblk = ch // LANES
    a0 = col0 // LANES
    tr = 256
    shift = CONV_PAD - (CONV_WIDTH - 1) // 2

    def body(a_ref, b_ref, w_ref, bias_ref, u0_ref, uc_ref, pad):
        z = jnp.zeros((CONV_PAD, LANES), F32)
        pad[0:CONV_PAD, :] = z
        pad[s + CONV_PAD: s + 2 * CONV_PAD, :] = z

        def glu(rows):
            u0 = a_ref[rows, :] * jax.nn.sigmoid(b_ref[rows, :])
            u0_ref[rows, :] = u0
            pad[pl.ds(rows.start + CONV_PAD, rows.size), :] = u0

        _row_chunks(s, glu)
        for t in range(0, s, tr):
            acc = jnp.broadcast_to(bias_ref[...], (tr, LANES))
            for k in range(CONV_WIDTH):
                acc = acc + w_ref[k:k + 1, :] * pad[t + k + shift: t + k + shift + tr, :]
            uc_ref[t:t + tr, :] = acc

    return _pallas(
        body, name=name, grid=(nblk,),
        in_specs=[pl.BlockSpec((s, LANES), lambda c: (0, a0 + c)),
                  pl.BlockSpec((s, LANES), lambda c: (0, a0 + nblk + c)),
                  pl.BlockSpec((CONV_WIDTH, LANES), lambda c: (0, c)),
                  pl.BlockSpec((1, LANES), lambda c: (0, c))],
        out_specs=[pl.BlockSpec((s, LANES), lambda c: (0, c))] * 2,
        out_shape=[jax.ShapeDtypeStruct((s, ch), F32)] * 2, operands=[proj, proj, conv_w, conv_b],
        scratch_shapes=[pltpu.VMEM((s + 2 * CONV_PAD, LANES), F32)], rider=rider)


def _ln_silu_fwd(uc, ln_w, ln_b, name):
    s, ch = uc.shape
    tm = 256

    def body(u_ref, w_ref, b_ref, o_ref):
        u = u_ref[...]
        mu = jnp.mean(u, axis=-1, keepdims=True)
        xc = u - mu
        rstd = lax.rsqrt(jnp.mean(xc * xc, axis=-1, keepdims=True) + EPS)
        z = xc * rstd * w_ref[...] + b_ref[...]
        o_ref[...] = (z * jax.nn.sigmoid(z)).astype(BF16)

    row = pl.BlockSpec((tm, ch), lambda i: (i, 0))
    vec = pl.BlockSpec((1, ch), lambda i: (0, 0))
    return pl.pallas_call(
        body, name=name, grid=(s // tm,), in_specs=[row, vec, vec], out_specs=row,
        out_shape=jax.ShapeDtypeStruct((s, ch), BF16), compiler_params=_params(),
    )(uc, ln_w, ln_b)


def _ln_silu_bwd(du3, uc, ln_w, ln_b, name):
    s, ch = uc.shape
    tm = 256

    def body(d_ref, u_ref, w_ref, b_ref, du_ref, dw_ref, db_ref):
        u = u_ref[...]
        mu = jnp.mean(u, axis=-1, keepdims=True)
        xc = u - mu
        rstd = lax.rsqrt(jnp.mean(xc * xc, axis=-1, keepdims=True) + EPS)
        xhat = xc * rstd
        z = xhat * w_ref[...] + b_ref[...]
        sg = jax.nn.sigmoid(z)
        dz = d_ref[...] * (sg * (1.0 + z * (1.0 - sg)))
        dxh = dz * w_ref[...]
        du_ref[...] = rstd * (dxh - jnp.mean(dxh, axis=-1, keepdims=True)
                              - xhat * jnp.mean(dxh * xhat, axis=-1, keepdims=True))
        pw = jnp.sum(dz * xhat, axis=0, keepdims=True)
        pb = jnp.sum(dz, axis=0, keepdims=True)
        first = pl.program_id(0) == 0

        @pl.when(first)
        def _():
            dw_ref[...] = pw
            db_ref[...] = pb

        @pl.when(jnp.logical_not(first))
        def _():
            dw_ref[...] += pw
            db_ref[...] += pb

    row = pl.BlockSpec((tm, ch), lambda i: (i, 0))
    vec = pl.BlockSpec((1, ch), lambda i: (0, 0))
    return pl.pallas_call(
        body, name=name, grid=(s // tm,), in_specs=[row, row, vec, vec], out_specs=[row, vec, vec],
        out_shape=[jax.ShapeDtypeStruct((s, ch), F32), jax.ShapeDtypeStruct((1, ch), F32),
                   jax.ShapeDtypeStruct((1, ch), F32)],
        compiler_params=_params(),
    )(du3, uc, ln_w, ln_b)


def _conv_bwd(duc, u0, proj, conv_w, col0, name, rider=None):
    s = proj.shape[0]
    ch = conv_w.shape[1]
    nblk = ch // LANES
    a0 = col0 // LANES
    tr = 256
    half = (CONV_WIDTH - 1) // 2
    shift = CONV_PAD - half

    def body(duc_ref, u0_ref, a_ref, b_ref, w_ref, da_ref, db_ref, dw_ref, dbias_ref, pad_d, pad_u):
        z = jnp.zeros((CONV_PAD, LANES), F32)
        for buf in (pad_d, pad_u):
            buf[0:CONV_PAD, :] = z
            buf[s + CONV_PAD: s + 2 * CONV_PAD, :] = z

        def fill(rows):
            dst = pl.ds(rows.start + CONV_PAD, rows.size)
            pad_d[dst, :] = duc_ref[rows, :]
            pad_u[dst, :] = u0_ref[rows, :]

        _row_chunks(s, fill)
        dw_acc = [jnp.zeros((8, LANES), F32) for _ in range(CONV_WIDTH)]
        dbias_acc = jnp.zeros((8, LANES), F32)
        for t in range(0, s, tr):
            d_t = duc_ref[t:t + tr, :]
            dbias_acc = dbias_acc + jnp.sum(d_t.reshape(tr // 8, 8, LANES), axis=0)
            du0 = jnp.zeros((tr, LANES), F32)
            for k in range(CONV_WIDTH):
                du0 = du0 + w_ref[k:k + 1, :] * pad_d[t - k + half + CONV_PAD: t - k + half + CONV_PAD + tr, :]
                prod = d_t * pad_u[t + k + shift: t + k + shift + tr, :]
                dw_acc[k] = dw_acc[k] + jnp.sum(prod.reshape(tr // 8, 8, LANES), axis=0)
            av = a_ref[t:t + tr, :]
            sg = jax.nn.sigmoid(b_ref[t:t + tr, :])
            da_ref[t:t + tr, :] = (du0 * sg).astype(BF16)
            db_ref[t:t + tr, :] = (du0 * av * sg * (1.0 - sg)).astype(BF16)
        for k in range(CONV_WIDTH):
            dw_ref[k:k + 1, :] = jnp.sum(dw_acc[k], axis=0, keepdims=True)
        dbias_ref[...] = jnp.sum(dbias_acc, axis=0, keepdims=True)

    col = lambda off: pl.BlockSpec((s, LANES), lambda c: (0, off + c))
    return _pallas(
        body, name=name, grid=(nblk,),
        in_specs=[col(0), col(0), col(a0), col(a0 + nblk),
                  pl.BlockSpec((CONV_WIDTH, LANES), lambda c: (0, c))],
        out_specs=[col(0), col(0), pl.BlockSpec((CONV_WIDTH, LANES), lambda c: (0, c)),
                   pl.BlockSpec((1, LANES), lambda c: (0, c))],
        out_shape=[jax.ShapeDtypeStruct((s, ch), BF16)] * 2
        + [jax.ShapeDtypeStruct((CONV_WIDTH, ch), F32), jax.ShapeDtypeStruct((1, ch), F32)],
        operands=[duc, u0, proj, proj, conv_w],
        scratch_shapes=[pltpu.VMEM((s + 2 * CONV_PAD, LANES), F32)] * 2, rider=rider)


GATE_BLK = 512


def _gate_fwd(proj, bg, y_a, y_b, col0, name):
    s, d = y_a.shape
    tm = 256
    g0 = col0 // GATE_BLK
    nb = d // GATE_BLK

    def body(ga_ref, gb_ref, ba_ref, bb_ref, ya_ref, yb_ref, o_ref):
        g_a = jax.nn.sigmoid(ga_ref[...] + ba_ref[...])
        g_b = jax.nn.sigmoid(gb_ref[...] + bb_ref[...])
        o_ref[...] = (g_a * ya_ref[...] + g_b * yb_ref[...]).astype(BF16)

    act = pl.BlockSpec((tm, GATE_BLK), lambda i, j: (i, j))
    return pl.pallas_call(
        body, name=name, grid=(s // tm, nb),
        in_specs=[pl.BlockSpec((tm, GATE_BLK), lambda i, j: (i, g0 + j)),
                  pl.BlockSpec((tm, GATE_BLK), lambda i, j: (i, g0 + nb + j)),
                  pl.BlockSpec((None, 1, GATE_BLK), lambda i, j: (0, 0, j)),
                  pl.BlockSpec((None, 1, GATE_BLK), lambda i, j: (1, 0, j)), act, act],
        out_specs=act, out_shape=jax.ShapeDtypeStruct((s, d), BF16), compiler_params=_params(),
    )(proj, proj, bg, bg, y_a, y_b)


def _out_proj_bwd_gates(dx1, w_out, proj, bg, y_a, y_b, col0, name, after=()):
    s, d = y_a.shape
    tm = 256
    half = d // 2
    assert col0 % half == 0
    c0 = col0 // half
    nt_dims = (((1,), (1,)), ((), ()))

    def body(dx_ref, w_ref, a0_ref, a1_ref, b0_ref, b1_ref, bias_ref, ya_ref, yb_ref,
             dgl_ref, dya_ref, dyb_ref, db_ref):
        dm = lax.dot_general(dx_ref[...], w_ref[...], nt_dims, preferred_element_type=F32)
        parts = []
        for br, (lo_ref, hi_ref, y_ref, dy_ref) in enumerate(((a0_ref, a1_ref, ya_ref, dya_ref),
                                                              (b0_ref, b1_ref, yb_ref, dyb_ref))):
            logits = jnp.concatenate([lo_ref[...], hi_ref[...]], axis=1)
            gate = jax.nn.sigmoid(logits + bias_ref[br])
            dy_ref[...] = (dm * gate).astype(BF16)
            dgl = dm * y_ref[...] * gate * (1.0 - gate)
            dgl_ref[:, br * d:(br + 1) * d] = dgl.astype(BF16)
            parts.append(jnp.sum(dgl, axis=0, keepdims=True))
        part = jnp.concatenate(parts, axis=0)
        first = pl.program_id(0) == 0

        @pl.when(first)
        def _():
            db_ref[...] = part

        @pl.when(jnp.logical_not(first))
        def _():
            db_ref[...] += part

    row = pl.BlockSpec((tm, d), lambda i: (i, 0))
    logit_blk = lambda k: pl.BlockSpec((tm, half), functools.partial(lambda i, k: (i, c0 + k), k=k))
    return _pallas(
        body, name=name, grid=(s // tm,),
        in_specs=[row, pl.BlockSpec((d, d), lambda i: (0, 0)), logit_blk(0), logit_blk(1), logit_blk(2),
                  logit_blk(3), pl.BlockSpec((2, 1, d), lambda i: (0, 0, 0)), row, row],
        out_specs=[pl.BlockSpec((tm, 2 * d), lambda i: (i, 0)), row, row, pl.BlockSpec((2, d), lambda i: (0, 0))],
        out_shape=[jax.ShapeDtypeStruct((s, 2 * d), BF16), jax.ShapeDtypeStruct((s, d), BF16),
                   jax.ShapeDtypeStruct((s, d), BF16), jax.ShapeDtypeStruct((2, d), F32)],
        operands=[dx1, w_out, proj, proj, proj, proj, bg, y_a, y_b], after=after)


def _ffn_in_swiglu(h2, w_blocked, name):
    s, k = h2.shape
    nblk, _, tn = w_blocked.shape
    ff = nblk // 2 * tn
    tm = 512

    def body(a_ref, wg_ref, wu_ref, g_ref, u_ref, act_ref):
        a = a_ref[...]
        gt = jnp.dot(a, wg_ref[...], preferred_element_type=F32)
        up = jnp.dot(a, wu_ref[...], preferred_element_type=F32)
        g_ref[...] = gt
        u_ref[...] = up
        act_ref[...] = (gt * jax.nn.sigmoid(gt) * up).astype(BF16)

    out = pl.BlockSpec((tm, tn), lambda j, i: (i, j))
    return pl.pallas_call(
        body, name=name, grid=(nblk // 2, s // tm),
        in_specs=[pl.BlockSpec((tm, k), lambda j, i: (i, 0)),
                  pl.BlockSpec((None, k, tn), lambda j, i: (j, 0, 0)),
                  pl.BlockSpec((None, k, tn), lambda j, i: (nblk // 2 + j, 0, 0))],
        out_specs=[out, out, out],
        out_shape=[jax.ShapeDtypeStruct((s, ff), F32), jax.ShapeDtypeStruct((s, ff), F32),
                   jax.ShapeDtypeStruct((s, ff), BF16)],
        compiler_params=_params(),
    )(h2, w_blocked, w_blocked)


def _ffn_out_bwd_swiglu(dy, w_ffn_out, gate, up, name, rider=None):
    s, d = dy.shape
    ff = gate.shape[1]
    tm = 256
    nt_dims = (((1,), (1,)), ((), ()))

    def body(dy_ref, w_ref, g_ref, u_ref, o_ref):
        dv = lax.dot_general(dy_ref[...], w_ref[...], nt_dims, preferred_element_type=F32)
        gt = g_ref[...]
        sg = jax.nn.sigmoid(gt)
        o_ref[:, 0:ff] = (dv * u_ref[...] * (sg * (1.0 + gt * (1.0 - sg)))).astype(BF16)
        o_ref[:, ff:2 * ff] = (dv * gt * sg).astype(BF16)

    row = pl.BlockSpec((tm, ff), lambda i: (i, 0))
    return _pallas(
        body, name=name, grid=(s // tm,),
        in_specs=[pl.BlockSpec((tm, d), lambda i: (i, 0)), pl.BlockSpec((ff, d), lambda i: (0, 0)), row, row],
        out_specs=pl.BlockSpec((tm, 2 * ff), lambda i: (i, 0)),
        out_shape=jax.ShapeDtypeStruct((s, 2 * ff), BF16), operands=[dy, w_ffn_out, gate, up], rider=rider)


def _out_proj_rmsnorm(mixed, w_out, x, norm_w, name):
    s, k = mixed.shape
    d = w_out.shape[1]
    tm = 512

    def body(a_ref, w_ref, x_ref, nw_ref, x1_ref, h2_ref):
        x1 = x_ref[...] + jnp.dot(a_ref[...], w_ref[...], preferred_element_type=F32)
        x1_ref[...] = x1
        rstd = lax.rsqrt(jnp.mean(x1 * x1, axis=-1, keepdims=True) + EPS)
        h2_ref[...] = (x1 * rstd * nw_ref[...]).astype(BF16)

    row = pl.BlockSpec((tm, d), lambda i: (i, 0))
    return pl.pallas_call(
        body, name=name, grid=(s // tm,),
        in_specs=[pl.BlockSpec((tm, k), lambda i: (i, 0)), pl.BlockSpec((k, d), lambda i: (0, 0)), row,
                  pl.BlockSpec((1, d), lambda i: (0, 0))],
        out_specs=[row, row],
        out_shape=[jax.ShapeDtypeStruct((s, d), F32), jax.ShapeDtypeStruct((s, d), BF16)],
        compiler_params=_params(),
    )(mixed, w_out, x, norm_w)


def _ffn_out_loss(act, w_ffn_out, x1, target, name):
    s, k = act.shape
    d = w_ffn_out.shape[1]
    tm = 512

    def body(a_ref, w_ref, x1_ref, t_ref, dy_ref, dyb_ref, loss_ref, acc):
        y = x1_ref[...] + jnp.dot(a_ref[...], w_ref[...], preferred_element_type=F32)
        diff = y - t_ref[...]
        dy = diff * (1.0 / d)
        dy_ref[...] = dy
        dyb_ref[...] = dy.astype(BF16)
        part = jnp.sum((diff * diff).reshape(tm // 8, 8, d), axis=0)
        i = pl.program_id(0)

        @pl.when(i == 0)
        def _():
            acc[...] = part

        @pl.when(i > 0)
        def _():
            acc[...] += part

        @pl.when(i == pl.num_programs(0) - 1)
        def _():
            loss_ref[...] = (0.5 / d) * jnp.sum(jnp.sum(acc[...], axis=1, keepdims=True), axis=0, keepdims=True)

    row = pl.BlockSpec((tm, d), lambda i: (i, 0))
    return pl.pallas_call(
        body, name=name, grid=(s // tm,),
        in_specs=[pl.BlockSpec((tm, k), lambda i: (i, 0)), pl.BlockSpec((k, d), lambda i: (0, 0)), row, row],
        out_specs=[row, row, pl.BlockSpec((1, 1), lambda i: (0, 0))],
        out_shape=[jax.ShapeDtypeStruct((s, d), F32), jax.ShapeDtypeStruct((s, d), BF16),
                   jax.ShapeDtypeStruct((1, 1), F32)],
        scratch_shapes=[pltpu.VMEM((8, d), F32)], compiler_params=_params(),
    )(act, w_ffn_out, x1, target)


LATE_GATHER = ("w_o_attn", "w_pw_conv", "w_out", "w_ffn_in", "w_ffn_out")
EARLY_REDUCE = LATE_GATHER


def _blocks_by_half(g):
    if g.ndim == 2:
        g = g.reshape(N_CHIPS, g.shape[0] // N_CHIPS, g.shape[1])
    return g.reshape(N_CHIPS, 2, g.shape[1] // 2, g.shape[2])


def _forward_backward(x, pos_col, target, wts, first_gather, late_bufs, pos_arr):
    wts = dict(wts)
    consts = _rope_consts()
    bd = consts[3]
    qw2 = jnp.tile(wts["q_norm_w"], (1, LANES // HEAD_DIM))
    kw2 = jnp.tile(wts["k_norm_w"], (1, LANES // HEAD_DIM))
    qkv_w = 3 * N_SLOT_HEADS * HEAD_DIM
    conv_col0 = 3 * qkv_w

    h = _rmsnorm_fwd(x, wts["norm1_w"], "rms1_fwd")
    slot_order = jnp.bitwise_xor(pos_arr[1], jnp.asarray([0, 2, 1, 3], jnp.int32))
    blocked = lambda buf: buf.reshape(N_CHIPS, -1, buf.shape[3])
    near = first_gather
    proj = _proj_by_slot(h, blocked(near[2][0]), slot_order, 0, 1, "mm_proj_own", after=list(late_bufs))
    near = _split_middle(near, after=[proj], name="allgather_w_in_near_forward")
    far, _ = _split_start(_gather_both_legs_rider(near[2][:1], near[2][1:], FAR_CHIPS), "allgather_w_in_far_start")
    _, bufs = _split_wait((near[0], near[1], far[2], near[3]), after=[], name="allgather_w_in_near_wait")
    proj = _proj_by_slot(h, blocked(bufs[0]), slot_order, 1, len(NEAR_CHIPS), "mm_proj_near", proj=proj)
    far = _split_middle((far[0], far[1], bufs, far[3]), after=[proj], name="allgather_w_in_far_forward")
    (w_in_buf, conv_w_buf, b_gate_buf), _ = _split_wait(far, after=[], name="allgather_w_in_far_wait")
    wts["w_in"] = w_in_buf.reshape(N_CHIPS, -1, w_in_buf.shape[3])
    wts["conv_w"] = conv_w_buf.transpose(1, 0, 2).reshape(CONV_WIDTH, -1)
    wts["b_gate"] = b_gate_buf.transpose(1, 0, 2).reshape(2, 1, -1)
    ch = wts["conv_w"].shape[1]
    gate_col0 = conv_col0 + 2 * ch
    gathering, started = _split_start(_gather_ici_rider(late_bufs, []), "late_gather_start", after=[wts["w_in"]])
    proj = _proj_by_slot(h, wts["w_in"], slot_order, 1 + len(NEAR_CHIPS), len(FAR_CHIPS), "mm_proj_far", proj=proj,
                         after=[started])
    qn, kn = _qk_fwd(proj, pos_col, qw2, kw2, consts, "qk_fwd")
    attn, lse, attn_b = _attn_fwd(qn, kn, proj, "attn_fwd")
    late_bufs, _ = _split_wait(gathering, after=[attn_b], name="late_gather_wait")
    (u0, uc), late_bufs = _conv_fwd(proj, wts["conv_w"], wts["conv_b"], conv_col0, "conv_fwd",
                                    rider=_gather_forward_rider(late_bufs))
    for n, buf in zip(LATE_GATHER, late_bufs):
        full = buf.reshape(N_CHIPS, -1, buf.shape[3])
        wts[n] = full.reshape(-1, full.shape[2]) if n in ROW_SHARDED else full
    y_a = _matmul(attn_b, wts["w_o_attn"], mode="nn", tm=1024, tn=256, tk=512, out_dtype=F32, name="mm_ya",
                  b_blocked=True)
    u3 = _ln_silu_fwd(uc, wts["conv_ln_w"], wts["conv_ln_b"], "ln_fwd")
    y_b = _matmul(u3, wts["w_pw_conv"], mode="nn", tm=1024, tn=256, tk=512, out_dtype=F32, name="mm_yb",
                  b_blocked=True)
    mixed = _gate_fwd(proj, wts["b_gate"], y_a, y_b, gate_col0, "gate_fwd")
    x1, h2 = _out_proj_rmsnorm(mixed, wts["w_out"], x, wts["norm2_w"], "mm_x1_rms2")
    gate, up, act = _ffn_in_swiglu(h2, wts["w_ffn_in"], "mm_gu_swiglu")
    dy, dy_b16, loss = _ffn_out_loss(act, wts["w_ffn_out"], x1, target, "mm_x2_loss")

    g = {}
    by_chip = {}

    def pair_add(n, blocks, received):
        return _add_own_half(blocks, received, pos_arr, f"grads_pair_add_{n}")

    g_ffn_out = _blocks_by_half(
        _matmul(act, dy_b16, mode="tn", tm=1408, tn=1024, tk=2048, out_dtype=F32, name="mm_dwffnout"))
    dgu, (received,) = _ffn_out_bwd_swiglu(dy_b16, wts["w_ffn_out"], gate, up, "mm_dact_swiglu_bwd",
                                           rider=_pair_exchange_rider([g_ffn_out], halved=True))
    to_send, own = pair_add("w_ffn_out", g_ffn_out, received)
    dh2, (by_chip["w_ffn_out"],) = _matmul(
        dgu, wts["w_ffn_in"], mode="nt", tm=1024, tn=1024, tk=1408, out_dtype=F32, name="mm_dh2", b_blocked=True,
        rider=_chip_exchange_rider([to_send], [own]))
    g_ffn_in = _blocks_by_half(_matmul(h2, dgu, mode="tn", tm=512, tn=1408, tk=2048, out_dtype=F32,
                                       name="mm_dwffnin", out_blocked=N_CHIPS, cols_outer=True))
    exchanging, started = _split_start(_pair_exchange_rider([g_ffn_in], halved=True), "grads_ffn_in_pair_start")
    dx1, dx1_b16, g["norm2_w"] = _rmsnorm_bwd(dh2, x1, wts["norm2_w"], dy, "rms2_bwd")
    g["w_out"] = _matmul(mixed, dx1_b16, mode="tn", tm=512, tn=1024, tk=2048, out_dtype=F32, name="mm_dwout",
                         after=[started])
    dgl, dy_a, dy_b, g["b_gate"] = _out_proj_bwd_gates(dx1_b16, wts["w_out"], proj, wts["b_gate"], y_a, y_b,
                                                       gate_col0, "mm_dmixed_gate_bwd")
    (received,), (g_ffn_in,) = _split_wait(exchanging, after=[dgl], name="grads_ffn_in_pair_wait")
    ffn_in_to_send, ffn_in_own = pair_add("w_ffn_in", g_ffn_in, received)
    dattn = _matmul(dy_a, wts["w_o_attn"], mode="nt", tm=1024, tn=512, tk=256, out_dtype=F32, name="mm_dattn",
                    b_blocked=True)
    g["w_o_attn"] = _matmul(attn_b, dy_a, mode="tn", tm=512, tn=256, tk=2048, out_dtype=F32, name="mm_dwo",
                            out_blocked=N_CHIPS)
    du3 = _matmul(dy_b, wts["w_pw_conv"], mode="nt", tm=1024, tn=512, tk=256, out_dtype=F32, name="mm_du3",
                  b_blocked=True)
    g["w_pw_conv"] = _matmul(u3, dy_b, mode="tn", tm=512, tn=256, tk=2048, out_dtype=F32, name="mm_dwpw",
                             out_blocked=N_CHIPS)
    duc, g["conv_ln_w"], g["conv_ln_b"] = _ln_silu_bwd(du3, uc, wts["conv_ln_w"], wts["conv_ln_b"], "ln_bwd")

    small3 = ("w_out", "w_o_attn", "w_pw_conv")
    g_small3 = [_blocks_by_half(g.pop(n)) for n in small3]
    (da, db, g["conv_w"], g["conv_b"]), received = _conv_bwd(
        duc, u0, proj, wts["conv_w"], conv_col0, "conv_bwd", rider=_pair_exchange_rider(g_small3, halved=True))
    sums3 = [pair_add(n, gb, rv) for n, gb, rv in zip(small3, g_small3, received)]
    (dqn, dkn, dv), (by_chip["w_ffn_in"],) = _attn_bwd(
        qn, kn, proj, dattn, attn, lse, bd, "attn_bwd",
        rider=_chip_exchange_rider([ffn_in_to_send], [ffn_in_own]))
    (dproj, dqw, dkw), exchanged3 = _qk_bwd(
        dqn, dkn, dv, da, db, dgl, proj, pos_col, qw2, kw2, consts, "qk_bwd",
        rider=_chip_exchange_rider([s[0] for s in sums3], [s[1] for s in sums3]))
    by_chip.update(zip(small3, exchanged3))
    halves = [_sum_chips(by_chip[n], pos_arr, f"grads_chip_sum_{n}") for n in EARLY_REDUCE]
    g["q_norm_w"] = dqw[:, :HEAD_DIM]
    g["k_norm_w"] = dkw[:, :HEAD_DIM]

    c = pos_arr[0]
    rh = h.shape[1] // 2
    h_sibling = lax.dynamic_slice_in_dim(h, (1 - c) * rh, rh, axis=1)
    h_own = lax.dynamic_slice_in_dim(h, c * rh, rh, axis=1)
    g_sibling, shards = _matmul(h_sibling, dproj, mode="tn", tm=rh, tn=1920, tk=2048, out_dtype=F32,
                                name="mm_dwin_sibling", out_blocked=N_CHIPS, rider=_pair_gather_rider(halves))
    reduced = dict(zip(EARLY_REDUCE, shards))
    exchanging, started = _split_start(_pair_exchange_rider([g_sibling], halved=False), "grads_w_in_pair_start")
    g_own = _matmul(h_own, dproj, mode="tn", tm=rh, tn=1920, tk=2048, out_dtype=F32, name="mm_dwin_own",
                    out_blocked=N_CHIPS, after=[started])
    (from_sibling,), _ = _split_wait(exchanging, after=[g_own], name="grads_w_in_pair_wait")
    to_send, own = _add_own_half(g_own, from_sibling, pos_arr, "grads_pair_add_w_in")
    in_flight, started = _split_start(_chip_exchange_rider([to_send], [own]), "grads_w_in_exchange_start")
    dh = _matmul(dproj, wts["w_in"], mode="nt", tm=1024, tn=1024, tk=1920, out_dtype=F32, name="mm_dh",
                 b_blocked=True, after=[started])
    grad_x, _, g["norm1_w"] = _rmsnorm_bwd(dh, x, wts["norm1_w"], dx1, "rms1_bwd")
    return loss, grad_x, g, reduced, (in_flight, started)


def _mesh_pos():
    return lax.axis_index("x"), lax.axis_index("y"), lax.axis_index("c")


def _other_chips(x, y):
    return [(1 - x, y), (x, 1 - y), (1 - x, 1 - y)]


NEAR_CHIPS, FAR_CHIPS = (0, 1), (2,)


def _cast_into_slot(shard, chip_arr, dtype, name, n_slots=N_CHIPS, after=()):
    r, c = shard.shape
    tr = r // 2 if r % 32 == 0 else r

    def body(chip_ref, s_ref, *refs):
        refs[-1][...] = s_ref[...].astype(dtype)

    return pl.pallas_call(
        body, name=name,
        grid_spec=pltpu.PrefetchScalarGridSpec(
            num_scalar_prefetch=1, grid=(r // tr,),
            in_specs=[pl.BlockSpec((tr, c), lambda i, chip_ref: (i, 0))] + [ANY] * len(after),
            out_specs=pl.BlockSpec((None, tr, c), lambda i, chip_ref: (chip_ref[0], i, 0))),
        out_shape=jax.ShapeDtypeStruct((n_slots, r, c), dtype), compiler_params=_params(),
    )(chip_arr, shard, *after)


GATHER_CHUNKS = 4


def _gather_both_legs_rider(big, small, peers):
    nb = len(big)
    n = nb + len(small)
    nch = GATHER_CHUNKS

    def part(bufs, a, slot, half, ch):
        if a >= nb:
            return bufs[a].at[slot]
        rows = bufs[a].shape[2] // nch
        return bufs[a].at[slot, half, pl.ds(ch * rows, rows)]

    def pieces():
        return [(a, ch, k) for ch in range(nch) for a in range(n) for k in peers if a < nb or ch == 0]

    def ici(bufs, sems, a, ch, k, slot_of_src):
        x, y, c = _mesh_pos()
        px, py = _other_chips(x, y)[k]
        slot = 2 * x + y if slot_of_src == "mine" else 2 * px + py
        return pltpu.make_async_remote_copy(
            src_ref=part(bufs, a, slot, c, ch), dst_ref=part(bufs, a, slot, c, ch), send_sem=sems[0].at[a, ch, k],
            recv_sem=sems[1].at[a, ch, k], device_id=(px, py, c), device_id_type=MESH)

    def forward(bufs, sems, a, ch, k, half):
        x, y, c = _mesh_pos()
        px, py = _other_chips(x, y)[k]
        h = c if half == "mine" else 1 - c
        return pltpu.make_async_remote_copy(
            src_ref=part(bufs, a, 2 * px + py, h, ch), dst_ref=part(bufs, a, 2 * px + py, h, ch),
            send_sem=sems[2].at[a, ch, k], recv_sem=sems[3].at[a, ch, k], device_id=(x, y, 1 - c),
            device_id_type=MESH)

    def start(r_in, bufs, sems):
        for a, ch, k in pieces():
            ici(bufs, sems, a, ch, k, "mine").start()

    def middle(r_in, bufs, sems):
        for a, ch, k in pieces():
            ici(bufs, sems, a, ch, k, "theirs").wait_recv()
            if a < nb:
                forward(bufs, sems, a, ch, k, "mine").start()
        for a, ch, k in pieces():
            ici(bufs, sems, a, ch, k, "mine").wait_send()

    def wait(r_in, bufs, sems):
        for a, ch, k in pieces():
            if a < nb:
                forward(bufs, sems, a, ch, k, "theirs").wait_recv()
        for a, ch, k in pieces():
            if a < nb:
                forward(bufs, sems, a, ch, k, "mine").wait_send()

    ops = list(big) + list(small)
    return _Rider(ops, [jax.ShapeDtypeStruct(o.shape, o.dtype) for o in ops], {i: i for i in range(n)},
                  [pltpu.SemaphoreType.DMA((n, nch, 3)), pltpu.SemaphoreType.DMA((n, nch, 3)),
                   pltpu.SemaphoreType.DMA((nb, nch, 3)), pltpu.SemaphoreType.DMA((nb, nch, 3))],
                  start, wait, middle)


def _comm_call(rider, name):
    def body():
        pass

    return _pallas(body, name=name, grid=(1,), in_specs=[], out_specs=[], out_shape=[], operands=[],
                   rider=rider)[1]


def _gather_ici_rider(big, small):
    nb = len(big)
    n = nb + len(small)

    def copies(bufs, sems):
        x, y, c = _mesh_pos()
        me = 2 * x + y
        part = lambda a, slot: bufs[a].at[slot, c] if a < nb else bufs[a].at[slot]
        out = []
        for a in range(n):
            for k, (px, py) in enumerate(_other_chips(x, y)):
                send = functools.partial(
                    pltpu.make_async_remote_copy,
                    src_ref=part(a, me), dst_ref=part(a, me), send_sem=sems[0].at[a, k],
                    recv_sem=sems[1].at[a, k], device_id=(px, py, c), device_id_type=MESH)
                recv = functools.partial(
                    pltpu.make_async_remote_copy,
                    src_ref=part(a, 2 * px + py), dst_ref=part(a, 2 * px + py), send_sem=sems[0].at[a, k],
                    recv_sem=sems[1].at[a, k], device_id=(px, py, c), device_id_type=MESH)
                out.append((send, recv))
        return out

    def start(r_in, r_out, sems):
        for send, _ in copies(r_out, sems):
            send().start()

    def wait(r_in, r_out, sems):
        cps = copies(r_out, sems)
        for _, recv in cps:
            recv().wait_recv()
        for send, _ in cps:
            send().wait_send()

    ops = list(big) + list(small)
    return _Rider(ops, [jax.ShapeDtypeStruct(o.shape, o.dtype) for o in ops], {i: i for i in range(n)},
                  [pltpu.SemaphoreType.DMA((n, 3)), pltpu.SemaphoreType.DMA((n, 3))], start, wait)


def _gather_forward_rider(big):
    n = len(big)

    def copies(bufs, sems):
        x, y, c = _mesh_pos()
        out = []
        for a in range(n):
            for k, (px, py) in enumerate(_other_chips(x, y)):
                slot = 2 * px + py
                send = functools.partial(
                    pltpu.make_async_remote_copy,
                    src_ref=bufs[a].at[slot, c], dst_ref=bufs[a].at[slot, c], send_sem=sems[0].at[a, k],
                    recv_sem=sems[1].at[a, k], device_id=(x, y, 1 - c), device_id_type=MESH)
                recv = functools.partial(
                    pltpu.make_async_remote_copy,
                    src_ref=bufs[a].at[slot, 1 - c], dst_ref=bufs[a].at[slot, 1 - c], send_sem=sems[0].at[a, k],
                    recv_sem=sems[1].at[a, k], device_id=(x, y, 1 - c), device_id_type=MESH)
                out.append((send, recv))
        return out

    def start(r_in, r_out, sems):
        for send, _ in copies(r_out, sems):
            send().start()

    def wait(r_in, r_out, sems):
        cps = copies(r_out, sems)
        for _, recv in cps:
            recv().wait_recv()
        for send, _ in cps:
            send().wait_send()

    return _Rider(big, [jax.ShapeDtypeStruct(o.shape, o.dtype) for o in big], {i: i for i in range(n)},
                  [pltpu.SemaphoreType.DMA((n, 3)), pltpu.SemaphoreType.DMA((n, 3))], start, wait)


def _pair_exchange_rider(gs, halved):
    n = len(gs)

    def copies(r_in, r_out, sems):
        x, y, c = _mesh_pos()
        return [pltpu.make_async_remote_copy(
            src_ref=r_in[a].at[:, 1 - c] if halved else r_in[a], dst_ref=r_out[a], send_sem=sems[0].at[a],
            recv_sem=sems[1].at[a], device_id=(x, y, 1 - c), device_id_type=MESH) for a in range(n)]

    def start(r_in, r_out, sems):
        for cp in copies(r_in, r_out, sems):
            cp.start()

    def wait(r_in, r_out, sems):
        for cp in copies(r_in, r_out, sems):
            cp.wait()

    return _Rider(gs, [jax.ShapeDtypeStruct((g.shape[0],) + g.shape[-2:], g.dtype) for g in gs], {},
                  [pltpu.SemaphoreType.DMA((n,)), pltpu.SemaphoreType.DMA((n,))], start, wait)


def _chip_exchange_rider(to_send, by_chip, row_range=None):
    n = len(to_send)

    def copies(r_in, r_out, sems):
        x, y, c = _mesh_pos()
        me = 2 * x + y
        rows = (lambda ref: ref) if row_range is None else (lambda ref: ref.at[pl.ds(*row_range)])
        out = []
        for a in range(n):
            for k, (px, py) in enumerate(_other_chips(x, y)):
                send = functools.partial(
                    pltpu.make_async_remote_copy,
                    src_ref=rows(r_in[a].at[2 * px + py]), dst_ref=rows(r_out[a].at[me]),
                    send_sem=sems[0].at[a, k], recv_sem=sems[1].at[a, k], device_id=(px, py, c),
                    device_id_type=MESH)
                recv = functools.partial(
                    pltpu.make_async_remote_copy,
                    src_ref=rows(r_in[a].at[me]), dst_ref=rows(r_out[a].at[2 * px + py]),
                    send_sem=sems[0].at[a, k], recv_sem=sems[1].at[a, k], device_id=(px, py, c),
                    device_id_type=MESH)
                out.append((send, recv))
        return out

    def start(r_in, r_out, sems):
        for send, _ in copies(r_in, r_out, sems):
            send().start()

    def wait(r_in, r_out, sems):
        cps = copies(r_in, r_out, sems)
        for _, recv in cps:
            recv().wait_recv()
        for send, _ in cps:
            send().wait_send()

    return _Rider(list(to_send) + list(by_chip), [jax.ShapeDtypeStruct(b.shape, b.dtype) for b in by_chip],
                  {n + i: i for i in range(n)},
                  [pltpu.SemaphoreType.DMA((n, 3)), pltpu.SemaphoreType.DMA((n, 3))], start, wait)


HBM = pl.BlockSpec(memory_space=pltpu.HBM)
SEM = pl.BlockSpec(memory_space=pltpu.SEMAPHORE)


_IN_FLIGHT = pltpu.CompilerParams(has_side_effects=pltpu.SideEffectType.DATAFLOW_SIDE_EFFECTING)


class _FlatSems:
    def __init__(self, ref, shape):
        self.ref, self.shape = ref, shape

    @property
    def at(self):
        return self

    def __getitem__(self, idx):
        idx = idx if isinstance(idx, tuple) else (idx,)
        flat = 0
        for i, n in zip(idx, self.shape):
            flat = flat * n + i
        return self.ref.at[flat]


def _flat_sem_types(rider):
    return tuple(pltpu.SemaphoreType.DMA((int(np.prod(s.shape)),)) for s in rider.scratch)


def _as_rider_sems(rider, refs):
    return [_FlatSems(r, s.shape) for r, s in zip(refs, rider.scratch)]


def _split_start(rider, name, after=()):
    n_in, n_out, n_sem = len(rider.operands), len(rider.out_shapes), len(rider.scratch)
    n_after = len(after)
    fresh = [j for j in range(n_out) if j not in rider.aliases.values()]
    by_out = {j: i for i, j in rider.aliases.items()}

    def body(*refs):
        r_in = refs[:n_in]
        refs = refs[n_in + n_after:]
        sems = refs[:n_sem]
        thru = refs[n_sem:n_sem + n_in]
        fresh_refs = refs[n_sem + n_in:n_sem + n_in + len(fresh)]
        token = refs[-1]
        r_out = [thru[by_out[j]] if j in by_out else fresh_refs[fresh.index(j)] for j in range(n_out)]
        rider.start(r_in, r_out, _as_rider_sems(rider, sems))
        token[...] = jnp.zeros_like(token)

    res = pl.pallas_call(
        body, name=name,
        out_shape=_flat_sem_types(rider) + tuple(pltpu.HBM(o.shape, o.dtype) for o in rider.operands)
        + tuple(pltpu.HBM(rider.out_shapes[j].shape, rider.out_shapes[j].dtype) for j in fresh)
        + (jax.ShapeDtypeStruct((8, LANES), F32),),
        in_specs=(HBM,) * n_in + (ANY,) * n_after,
        out_specs=(SEM,) * n_sem + (HBM,) * (n_in + len(fresh)) + (pl.BlockSpec(memory_space=pltpu.VMEM),),
        input_output_aliases={i: n_sem + i for i in range(n_in)}, compiler_params=_IN_FLIGHT,
    )(*[pltpu.with_memory_space_constraint(o, pltpu.HBM) for o in rider.operands], *after)
    return (rider, res[:n_sem], res[n_sem:n_sem + n_in], res[n_sem + n_in:-1]), res[-1]


def _split_continue(handles, after, name, phase):
    rider, sems, thru, fresh_arrays = handles
    n_in, n_out, n_sem = len(rider.operands), len(rider.out_shapes), len(rider.scratch)
    fresh = [j for j in range(n_out) if j not in rider.aliases.values()]
    by_out = {j: i for i, j in rider.aliases.items()}
    n_data = n_in + len(fresh)

    def body(*refs):
        r_in = refs[:n_in]
        fresh_refs = refs[n_in:n_data]
        sem_refs = refs[n_data:n_data + n_sem]
        r_out = [r_in[by_out[j]] if j in by_out else fresh_refs[fresh.index(j)] for j in range(n_out)]
        phase(r_in, r_out, _as_rider_sems(rider, sem_refs))

    data = list(thru) + list(fresh_arrays)
    return pl.pallas_call(
        body, name=name, out_shape=tuple(pltpu.HBM(d.shape, d.dtype) for d in data),
        in_specs=(HBM,) * n_data + (SEM,) * n_sem + (ANY,) * len(after), out_specs=(HBM,) * n_data,
        input_output_aliases={i: i for i in range(n_data)}, compiler_params=_IN_FLIGHT,
    )(*data, *sems, *after)


def _split_middle(handles, after, name):
    rider, sems, thru, _ = handles
    res = _split_continue(handles, after, name, rider.middle)
    return rider, sems, res[:len(thru)], res[len(thru):]


def _split_wait(handles, after, name):
    rider = handles[0]
    n_in, n_out = len(rider.operands), len(rider.out_shapes)
    fresh = [j for j in range(n_out) if j not in rider.aliases.values()]
    by_out = {j: i for i, j in rider.aliases.items()}
    res = _split_continue(handles, after, name, rider.wait)
    return [res[by_out[j]] if j in by_out else res[n_in + fresh.index(j)] for j in range(n_out)], res[:n_in]


def _pair_gather_rider(bufs):
    n = len(bufs)

    def copies(r_out, sems):
        x, y, c = _mesh_pos()
        out = []
        for a in range(n):
            send = functools.partial(
                    pltpu.make_async_remote_copy,
                src_ref=r_out[a].at[c], dst_ref=r_out[a].at[c], send_sem=sems[0].at[a],
                recv_sem=sems[1].at[a], device_id=(x, y, 1 - c), device_id_type=MESH)
            recv = functools.partial(
                    pltpu.make_async_remote_copy,
                src_ref=r_out[a].at[1 - c], dst_ref=r_out[a].at[1 - c], send_sem=sems[0].at[a],
                recv_sem=sems[1].at[a], device_id=(x, y, 1 - c), device_id_type=MESH)
            out.append((send, recv))
        return out

    def start(r_in, r_out, sems):
        for send, _ in copies(r_out, sems):
            send().start()

    def wait(r_in, r_out, sems):
        cps = copies(r_out, sems)
        for _, recv in cps:
            recv().wait_recv()
        for send, _ in cps:
            send().wait_send()

    return _Rider(bufs, [jax.ShapeDtypeStruct(b.shape, b.dtype) for b in bufs], {i: i for i in range(n)},
                  [pltpu.SemaphoreType.DMA((n,)), pltpu.SemaphoreType.DMA((n,))], start, wait)


def _add_own_half(g, recv, pos_arr, name):
    nb, rh, cols = g.shape[0], g.shape[-2], g.shape[-1]

    def body(pos_ref, g_ref, r_ref, send_ref, own_ref):
        s = (g_ref[...] + r_ref[...]).astype(BF16)
        send_ref[...] = s

        @pl.when(pl.program_id(0) == pos_ref[1])
        def _():
            own_ref[...] = s

    blk = pl.BlockSpec((None, rh, cols), lambda j, pos_ref: (j, 0, 0))
    g_spec = blk if g.ndim == 3 else pl.BlockSpec((None, None, rh, cols),
                                                   lambda j, pos_ref: (j, pos_ref[0], 0, 0))
    shape = jax.ShapeDtypeStruct((nb, rh, cols), BF16)
    return pl.pallas_call(
        body, name=name,
        grid_spec=pltpu.PrefetchScalarGridSpec(
            num_scalar_prefetch=1, grid=(nb,), in_specs=[g_spec, blk],
            out_specs=[blk, pl.BlockSpec((None, rh, cols), lambda j, pos_ref: (pos_ref[1], 0, 0))]),
        out_shape=[shape, shape], compiler_params=_params(),
    )(pos_arr, g, recv)


def _sum_chips(gath, pos_arr, name):
    nb, rh, cols = gath.shape

    def body(pos_ref, a_ref, b_ref, c_ref, d_ref, o_ref):
        del pos_ref
        o_ref[...] = ((a_ref[...].astype(F32) + b_ref[...].astype(F32)) + c_ref[...].astype(F32)) \
            + d_ref[...].astype(F32)

    tr = rh // 2 if (rh // 2) % 16 == 0 else rh
    specs = [pl.BlockSpec((None, tr, cols), functools.partial(lambda i, pos_ref, j: (j, i, 0), j=j))
             for j in range(nb)]
    return pl.pallas_call(
        body, name=name,
        grid_spec=pltpu.PrefetchScalarGridSpec(
            num_scalar_prefetch=1, grid=(rh // tr,), in_specs=specs,
            out_specs=pl.BlockSpec((None, tr, cols), lambda i, pos_ref: (pos_ref[0], i, 0))),
        out_shape=jax.ShapeDtypeStruct((2, rh, cols), F32), compiler_params=_params(),
    )(pos_arr, gath, gath, gath, gath)


N_DEVICES = 8


def _small_gather_rider(buf):
    def copies(r_out, sems):
        x, y, c = _mesh_pos()
        me = 4 * x + 2 * y + c
        out = []
        for r in range(1, N_DEVICES):
            px = 1 - x if r & 4 else x
            py = 1 - y if r & 2 else y
            pc = 1 - c if r & 1 else c
            out.append(pltpu.make_async_remote_copy(
                src_ref=r_out[0].at[me], dst_ref=r_out[0].at[me], send_sem=sems[0].at[r - 1],
                recv_sem=sems[1].at[r - 1], device_id=(px, py, pc), device_id_type=MESH))
        return out

    def start(r_in, r_out, sems):
        for cp in copies(r_out, sems):
            cp.start()

    def wait(r_in, r_out, sems):
        cps = copies(r_out, sems)
        for cp in cps:
            cp.wait_recv()
        for cp in cps:
            cp.wait_send()

    return _Rider([buf], [jax.ShapeDtypeStruct(buf.shape, buf.dtype)], {0: 0},
                  [pltpu.SemaphoreType.DMA((N_DEVICES - 1,)), pltpu.SemaphoreType.DMA((N_DEVICES - 1,))],
                  start, wait)


def _sum_devices(buf, name):
    def body(b_ref, o_ref):
        acc = b_ref[0]
        for i in range(1, N_DEVICES):
            acc = acc + b_ref[i]
        o_ref[...] = acc

    return _pallas(body, name=name, grid=(1,), in_specs=[pl.BlockSpec(buf.shape, lambda i: (0, 0, 0))],
                   out_specs=pl.BlockSpec(buf.shape[1:], lambda i: (0, 0)),
                   out_shape=jax.ShapeDtypeStruct(buf.shape[1:], F32), operands=[buf])


def _adamw_math(w, g, m, v):
    m = ADAM_B1 * m + (1.0 - ADAM_B1) * g
    v = ADAM_B2 * v + (1.0 - ADAM_B2) * (g * g)
    m_hat = m / (1.0 - ADAM_B1 ** ADAM_STEP)
    v_hat = v / (1.0 - ADAM_B2 ** ADAM_STEP)
    delta = -ADAM_LR * (m_hat / (jnp.sqrt(v_hat) + ADAM_EPS) + ADAM_WD * w)
    return delta, m, v


def _adamw(w, g, m, v, name, after=()):
    r, c = w.shape
    tr = 128 if r % 128 == 0 else 64
    assert r % tr == 0

    def body(w_ref, g_ref, m_ref, v_ref, go_ref, d_ref, mo_ref, vo_ref):
        gv = g_ref[...]
        d, mn, vn = _adamw_math(w_ref[...], gv, m_ref[...], v_ref[...])
        go_ref[...] = gv
        d_ref[...] = d
        mo_ref[...] = mn
        vo_ref[...] = vn

    blk = pl.BlockSpec((tr, c), lambda i: (i, 0))
    return _pallas(body, name=name, grid=(r // tr,), in_specs=[blk] * 4, out_specs=[blk] * 4,
                   out_shape=[jax.ShapeDtypeStruct((r, c), F32)] * 4, operands=[w, g, m, v], after=after)


def _adamw_small(ws, gs, ms, vs, name):
    n = len(ws)

    def body(*refs):
        w_r, g_r, m_r, v_r = refs[:n], refs[n:2 * n], refs[2 * n:3 * n], refs[3 * n:4 * n]
        d_o, m_o, v_o = refs[4 * n:5 * n], refs[5 * n:6 * n], refs[6 * n:7 * n]
        for i in range(n):
            d, mn, vn = _adamw_math(w_r[i][...], g_r[i][...], m_r[i][...], v_r[i][...])
            d_o[i][...] = d
            m_o[i][...] = mn
            v_o[i][...] = vn

    specs = [pl.BlockSpec(w.shape, lambda i: (0, 0)) for w in ws]
    shapes = [jax.ShapeDtypeStruct(w.shape, F32) for w in ws]
    outs = pl.pallas_call(
        body, name=name, grid=(1,), in_specs=specs * 4, out_specs=specs * 3, out_shape=shapes * 3,
        compiler_params=_params(),
    )(*ws, *gs, *ms, *vs)
    return outs[:n], outs[n:2 * n], outs[2 * n:]


BIG = ("w_in", "w_o_attn", "w_pw_conv", "w_out", "w_ffn_in", "w_ffn_out")
ROW_SHARDED = ("w_out", "w_ffn_out")
SMALL = ("norm1_w", "b_gate", "q_norm_w", "k_norm_w", "conv_w", "conv_b", "conv_ln_w", "conv_ln_b", "norm2_w")
ORDER = ("norm1_w", "w_in", "b_gate", "q_norm_w", "k_norm_w", "w_o_attn", "conv_w", "conv_b", "conv_ln_w",
         "conv_ln_b", "w_pw_conv", "w_out", "norm2_w", "w_ffn_in", "w_ffn_out")
PACK_TILE = 8 * LANES


def _pack_small(parts):
    rows = []
    for p in parts:
        flat = p.reshape(-1)
        pad = (-flat.shape[0]) % PACK_TILE
        rows.append(jnp.pad(flat, (0, pad)).reshape(-1, LANES))
    return jnp.concatenate(rows, axis=0)


def _unpack_small(packed, shapes):
    out, row = [], 0
    for shp in shapes:
        size = int(np.prod(shp))
        nrow = -(-size // PACK_TILE) * (PACK_TILE // LANES)
        out.append(packed[row:row + nrow].reshape(-1)[:size].reshape(shp))
        row += nrow
    return out


def kernel(x, positions, norm1_w, w_in, b_gate, q_norm_w, k_norm_w, w_o_attn, conv_w, conv_b, conv_ln_w, conv_ln_b, w_pw_conv, w_out, norm2_w, w_ffn_in, w_ffn_out, loss_target, m_norm1_w, m_w_in, m_b_gate, m_q_norm_w, m_k_norm_w, m_w_o_attn, m_conv_w, m_conv_b, m_conv_ln_w, m_conv_ln_b, m_w_pw_conv, m_w_out, m_norm2_w, m_w_ffn_in, m_w_ffn_out, v_norm1_w, v_w_in, v_b_gate, v_q_norm_w, v_k_norm_w, v_w_o_attn, v_conv_w, v_conv_b, v_conv_ln_w, v_conv_ln_b, v_w_pw_conv, v_w_out, v_norm2_w, v_w_ffn_in, v_w_ffn_out):
    w = dict(norm1_w=norm1_w, w_in=w_in, b_gate=b_gate, q_norm_w=q_norm_w, k_norm_w=k_norm_w, w_o_attn=w_o_attn,
             conv_w=conv_w, conv_b=conv_b, conv_ln_w=conv_ln_w, conv_ln_b=conv_ln_b, w_pw_conv=w_pw_conv,
             w_out=w_out, norm2_w=norm2_w, w_ffn_in=w_ffn_in, w_ffn_out=w_ffn_out)
    m = dict(norm1_w=m_norm1_w, w_in=m_w_in, b_gate=m_b_gate, q_norm_w=m_q_norm_w, k_norm_w=m_k_norm_w,
             w_o_attn=m_w_o_attn, conv_w=m_conv_w, conv_b=m_conv_b, conv_ln_w=m_conv_ln_w,
             conv_ln_b=m_conv_ln_b, w_pw_conv=m_w_pw_conv, w_out=m_w_out, norm2_w=m_norm2_w,
             w_ffn_in=m_w_ffn_in, w_ffn_out=m_w_ffn_out)
    v = dict(norm1_w=v_norm1_w, w_in=v_w_in, b_gate=v_b_gate, q_norm_w=v_q_norm_w, k_norm_w=v_k_norm_w,
             w_o_attn=v_w_o_attn, conv_w=v_conv_w, conv_b=v_conv_b, conv_ln_w=v_conv_ln_w,
             conv_ln_b=v_conv_ln_b, w_pw_conv=v_w_pw_conv, w_out=v_w_out, norm2_w=v_norm2_w,
             w_ffn_in=v_w_ffn_in, w_ffn_out=v_w_ffn_out)
    cx, cy, cc = _mesh_pos()
    chip = 2 * cx + cy

    chip_arr = chip.reshape(1).astype(jnp.int32)
    pos_arr = jnp.stack([cc, chip]).astype(jnp.int32)
    halves = lambda buf: buf.reshape(N_CHIPS, 2, buf.shape[1] // 2, buf.shape[2])
    w_in_buf = halves(_cast_into_slot(w["w_in"][0], chip_arr, BF16, "cast_w_in"))
    small_bufs = [_cast_into_slot(w[n][0], chip_arr, F32, f"slot_{n}") for n in ("conv_w", "b_gate")]
    first_gather, started = _split_start(_gather_both_legs_rider([w_in_buf], small_bufs, NEAR_CHIPS),
                                         "allgather_w_in_near_start")
    late_bufs = [halves(_cast_into_slot(w[n][0], chip_arr, BF16, f"cast_{n}", after=[started]))
                 for n in LATE_GATHER]
    wts = dict(norm1_w=norm1_w, q_norm_w=q_norm_w, k_norm_w=k_norm_w, conv_b=conv_b, conv_ln_w=conv_ln_w,
               conv_ln_b=conv_ln_b, norm2_w=norm2_w)

    loss, grad_x, g, reduced, w_in_in_flight = _forward_backward(
        x[0], positions.reshape(-1, 1), loss_target[0], wts, first_gather, late_bufs, pos_arr)
    grads = {n: b.reshape(-1, b.shape[2]) for n, b in reduced.items()}

    w_in_in_flight, started = w_in_in_flight
    delta, new_m, new_v = {}, {}, {}
    for n in EARLY_REDUCE:
        grads[n], delta[n], new_m[n], new_v[n] = _adamw(w[n][0], grads[n], m[n][0], v[n][0], f"adamw_{n}",
                                                        after=[started])
    small_parts = [loss] + [g[n] for n in SMALL]
    small_shapes = [p.shape for p in small_parts]
    device_arr = (4 * cx + 2 * cy + cc).reshape(1).astype(jnp.int32)
    small_buf = _cast_into_slot(_pack_small(small_parts), device_arr, F32, "slot_small", n_slots=N_DEVICES)

    (by_chip_w_in,), _ = _split_wait(w_in_in_flight, after=[delta[n] for n in EARLY_REDUCE] + [small_buf],
                                     name="grads_w_in_exchange_wait")
    half_w_in = _sum_chips(by_chip_w_in, pos_arr, "grads_chip_sum_w_in")
    shard_w_in, small_buf = _comm_call(
        _riders_together(_pair_gather_rider([half_w_in]), _small_gather_rider(small_buf)),
        "grads_pair_gather_w_in_small_gather")
    summed = _sum_devices(small_buf, "small_sum")
    reduced = _unpack_small(summed, small_shapes)
    loss_total = reduced[0].reshape(())
    for n, r in zip(SMALL, reduced[1:]):
        grads[n] = r
    ch_shard = conv_w.shape[2]
    grads["conv_w"] = lax.dynamic_slice_in_dim(grads["conv_w"], chip * ch_shard, ch_shard, axis=1)
    d_shard = b_gate.shape[2]
    grads["b_gate"] = lax.dynamic_slice_in_dim(grads["b_gate"], chip * d_shard, d_shard, axis=1)

    grads["w_in"], delta["w_in"], new_m["w_in"], new_v["w_in"] = _adamw(
        w["w_in"][0], shard_w_in.reshape(-1, shard_w_in.shape[2]), m["w_in"][0], v["w_in"][0], "adamw_w_in")
    flat2 = lambda a: a.reshape(-1, a.shape[-1])
    d_s, m_s, v_s = _adamw_small([flat2(w[n]) for n in SMALL], [flat2(grads[n]) for n in SMALL],
                                 [flat2(m[n]) for n in SMALL], [flat2(v[n]) for n in SMALL], "adamw_small")
    for i, n in enumerate(SMALL):
        delta[n], new_m[n], new_v[n] = d_s[i], m_s[i], v_s[i]

    shaped = lambda d, n: d[n].reshape(w[n].shape)
    return (loss_total, grad_x[None], *[shaped(grads, n) for n in ORDER], *[shaped(delta, n) for n in ORDER],
            *[shaped(new_m, n) for n in ORDER], *[shaped(new_v, n) for n in ORDER])
```

```python
import functools

import numpy as np
import jax
import jax.numpy as jnp
from jax import lax
from jax.experimental import pallas as pl
from jax.experimental.pallas import tpu as pltpu

F32 = jnp.float32
BF16 = jnp.bfloat16
MESH = pl.DeviceIdType.MESH
ANY = pl.BlockSpec(memory_space=pl.ANY)

HEAD_DIM = 64
N_SLOT_HEADS = 8
DILATIONS = (1, 4, 16)
HALF_SPAN = 64
ROPE_THETA = 500000.0
ROT_DIM = 16
CONV_WIDTH = 31
EPS = 1e-6
NEG_INF = -1e30
ADAM_LR, ADAM_B1, ADAM_B2, ADAM_EPS, ADAM_WD, ADAM_STEP = 0.001, 0.9, 0.999, 1e-08, 0.01, 10

LANES = 128
QBLK = 128
KWIN = QBLK + 2 * HALF_SPAN
VMEM_LIMIT = 48 * 1024 * 1024
N_CHIPS = 4


def _params(**kw):
    return pltpu.CompilerParams(vmem_limit_bytes=VMEM_LIMIT, **kw)


class _Rider:
    def __init__(self, operands, out_shapes, aliases, scratch, start, wait, middle=None):
        self.operands, self.out_shapes, self.aliases = list(operands), list(out_shapes), dict(aliases)
        self.scratch, self.start, self.wait = list(scratch), start, wait
        self.middle = middle


def _riders_together(a, b):
    n_in, n_out, n_sc = len(a.operands), len(a.out_shapes), len(a.scratch)
    aliases = dict(a.aliases)
    aliases.update({n_in + src: n_out + dst for src, dst in b.aliases.items()})

    def start(r_in, r_out, r_sc):
        a.start(r_in[:n_in], r_out[:n_out], r_sc[:n_sc])
        b.start(r_in[n_in:], r_out[n_out:], r_sc[n_sc:])

    def wait(r_in, r_out, r_sc):
        a.wait(r_in[:n_in], r_out[:n_out], r_sc[:n_sc])
        b.wait(r_in[n_in:], r_out[n_out:], r_sc[n_sc:])

    return _Rider(a.operands + b.operands, a.out_shapes + b.out_shapes, aliases, a.scratch + b.scratch, start, wait)


def _pallas(body, *, name, grid, in_specs, out_specs, out_shape, operands, scratch_shapes=(), aliases=None,
            rider=None, after=()):
    single = not isinstance(out_specs, (list, tuple))
    out_specs_l = [out_specs] if single else list(out_specs)
    out_shape_l = [out_shape] if single else list(out_shape)
    aliases = dict(aliases or {})

    def call(fn, all_in_specs, all_out_specs, all_out_shape, all_scratch, all_aliases, all_operands):
        return pl.pallas_call(
            fn, name=name, grid=grid, in_specs=all_in_specs, out_specs=all_out_specs, out_shape=all_out_shape,
            scratch_shapes=all_scratch, input_output_aliases=all_aliases, compiler_params=_params(),
        )(*all_operands)

    if rider is None:
        n_main = len(in_specs)

        def ordered(*refs):
            body(*refs[:n_main], *refs[n_main + len(after):])

        res = call(ordered if after else body, list(in_specs) + [ANY] * len(after), out_specs_l, out_shape_l,
                   list(scratch_shapes), aliases, list(operands) + list(after))
        return res[0] if single else res
    assert not after
    n_in, n_rin = len(in_specs), len(rider.operands)
    n_out, n_rout = len(out_specs_l), len(rider.out_shapes)
    n_sc = len(scratch_shapes)

    def wrapped(*refs):
        main_in, r_in = refs[:n_in], refs[n_in:n_in + n_rin]
        o0 = n_in + n_rin
        main_out, r_out = refs[o0:o0 + n_out], refs[o0 + n_out:o0 + n_out + n_rout]
        s0 = o0 + n_out + n_rout
        main_sc, r_sc = refs[s0:s0 + n_sc], refs[s0 + n_sc:]
        ids = [pl.program_id(d) for d in range(len(grid))]
        first = functools.reduce(jnp.logical_and, [i == 0 for i in ids])
        last = functools.reduce(jnp.logical_and, [i == n - 1 for i, n in zip(ids, grid)])

        @pl.when(first)
        def _():
            rider.start(r_in, r_out, r_sc)

        body(*main_in, *main_out, *main_sc)

        @pl.when(last)
        def _():
            rider.wait(r_in, r_out, r_sc)

    for src, dst in rider.aliases.items():
        aliases[n_in + src] = n_out + dst
    res = call(wrapped, list(in_specs) + [ANY] * n_rin, out_specs_l + [ANY] * n_rout,
               out_shape_l + rider.out_shapes, list(scratch_shapes) + rider.scratch, aliases,
               list(operands) + rider.operands)
    main = res[:n_out]
    return (main[0] if single else main), res[n_out:]


def _matmul(a, b, *, mode, tm, tn, tk, out_dtype, name, b_blocked=False,
            out_blocked=None, cols_outer=False, rider=None, after=()):
    a_shape = a.shape
    if mode == "nn":
        m_dim, k_dim = a_shape
        n_dim = b.shape[0] * b.shape[2] if b_blocked else b.shape[1]
        rows, cols, red = m_dim, n_dim, k_dim
    elif mode == "nt":
        m_dim, n_dim = a_shape
        k_dim = b.shape[1] if b_blocked else b.shape[0]
        rows, cols, red = m_dim, k_dim, n_dim
    else:
        m_dim, k_dim = a_shape
        n_dim = b.shape[1]
        rows, cols, red = k_dim, n_dim, m_dim
    assert rows % tm == 0 and cols % tn == 0 and red % tk == 0, (name, rows, cols, red)
    ni, nj, nk = rows // tm, cols // tn, red // tk

    if mode == "nn":
        a_spec = pl.BlockSpec((tm, tk), lambda i, j, k: (i, k))
        if b_blocked:
            per = b.shape[2] // tn
            b_spec = pl.BlockSpec((None, tk, tn), lambda i, j, k: (j // per, k, j % per))
        else:
            b_spec = pl.BlockSpec((tk, tn), lambda i, j, k: (k, j))
        dims = (((1,), (0,)), ((), ()))
    elif mode == "nt":
        a_spec = pl.BlockSpec((tm, tk), lambda i, j, k: (i, k))
        if b_blocked:
            per = b.shape[2] // tk
            b_spec = pl.BlockSpec((None, tn, tk), lambda i, j, k: (k // per, j, k % per))
        else:
            b_spec = pl.BlockSpec((tn, tk), lambda i, j, k: (j, k))
        dims = (((1,), (1,)), ((), ()))
    else:
        a_spec = pl.BlockSpec((tk, tm), lambda i, j, k: (k, i))
        b_spec = pl.BlockSpec((tk, tn), lambda i, j, k: (k, j))
        dims = (((0,), (0,)), ((), ()))

    if out_blocked:
        per_o = (cols // out_blocked) // tn
        out_spec = pl.BlockSpec((None, tm, tn), lambda i, j, k: (j // per_o, i, j % per_o))
        out_shape = jax.ShapeDtypeStruct((out_blocked, rows, cols // out_blocked), out_dtype)
    else:
        out_spec = pl.BlockSpec((tm, tn), lambda i, j, k: (i, j))
        out_shape = jax.ShapeDtypeStruct((rows, cols), out_dtype)

    def body(a_ref, b_ref, o_ref, *acc):
        prod = lax.dot_general(a_ref[...], b_ref[...], dims, preferred_element_type=F32)
        if nk == 1:
            o_ref[...] = prod.astype(out_dtype)
        else:
            acc_ref, = acc
            k = pl.program_id(2)

            @pl.when(k == 0)
            def _():
                acc_ref[...] = prod

            @pl.when(k > 0)
            def _():
                acc_ref[...] += prod

            @pl.when(k == nk - 1)
            def _():
                o_ref[...] = acc_ref[...].astype(out_dtype)

    scratch = [pltpu.VMEM((tm, tn), F32)] if nk > 1 else []
    grid = (ni, nj, nk)
    if cols_outer:
        swap = lambda spec: pl.BlockSpec(spec.block_shape, lambda j, i, k, f=spec.index_map: f(i, j, k))
        a_spec, b_spec, out_spec, grid = swap(a_spec), swap(b_spec), swap(out_spec), (nj, ni, nk)
    return _pallas(body, name=name, grid=grid, in_specs=[a_spec, b_spec], out_specs=out_spec,
                   out_shape=out_shape, operands=[a, b], scratch_shapes=scratch, rider=rider, after=after)


def _proj_by_slot(h, w_in, order, first, count, name, proj=None, after=()):
    s, k = h.shape
    nb = w_in.shape[2]
    tm = 512
    prev = [] if proj is None else [proj]

    def body(order_ref, h_ref, w_ref, *refs):
        refs[-1][...] = jnp.dot(h_ref[...], w_ref[...], preferred_element_type=F32)

    return pl.pallas_call(
        body, name=name,
        grid_spec=pltpu.PrefetchScalarGridSpec(
            num_scalar_prefetch=1, grid=(count, s // tm),
            in_specs=[pl.BlockSpec((tm, k), lambda j, i, order_ref: (i, 0)),
                      pl.BlockSpec((None, k, nb), lambda j, i, order_ref: (order_ref[first + j], 0, 0))]
            + [ANY] * (len(prev) + len(after)),
            out_specs=pl.BlockSpec((tm, nb), lambda j, i, order_ref: (i, order_ref[first + j]))),
        out_shape=jax.ShapeDtypeStruct((s, N_CHIPS * nb), F32),
        input_output_aliases={3: 0} if prev else {}, compiler_params=_params(),
    )(order, h, w_in, *prev, *after)


def _rmsnorm_fwd(x, w, name):
    s, d = x.shape
    tm = 256

    def body(x_ref, w_ref, o_ref):
        xv = x_ref[...]
        rstd = lax.rsqrt(jnp.mean(xv * xv, axis=-1, keepdims=True) + EPS)
        o_ref[...] = (xv * rstd * w_ref[...]).astype(BF16)

    return pl.pallas_call(
        body, name=name, grid=(s // tm,),
        in_specs=[pl.BlockSpec((tm, d), lambda i: (i, 0)), pl.BlockSpec((1, d), lambda i: (0, 0))],
        out_specs=pl.BlockSpec((tm, d), lambda i: (i, 0)),
        out_shape=jax.ShapeDtypeStruct((s, d), BF16), compiler_params=_params(),
    )(x, w)


def _rmsnorm_bwd(dh, x, w, dres, name, rider=None):
    s, d = x.shape
    tm = 256

    def body(dh_ref, x_ref, w_ref, dres_ref, dx_ref, dxb_ref, dw_ref):
        xv = x_ref[...]
        rstd = lax.rsqrt(jnp.mean(xv * xv, axis=-1, keepdims=True) + EPS)
        xhat = xv * rstd
        dhv = dh_ref[...]
        g = dhv * w_ref[...]
        dx = rstd * (g - xhat * jnp.mean(g * xhat, axis=-1, keepdims=True)) + dres_ref[...]
        dx_ref[...] = dx
        dxb_ref[...] = dx.astype(BF16)
        part = jnp.sum(dhv * xhat, axis=0, keepdims=True)

        @pl.when(pl.program_id(0) == 0)
        def _():
            dw_ref[...] = part

        @pl.when(pl.program_id(0) > 0)
        def _():
            dw_ref[...] += part

    row = pl.BlockSpec((tm, d), lambda i: (i, 0))
    vec = pl.BlockSpec((1, d), lambda i: (0, 0))
    return _pallas(
        body, name=name, grid=(s // tm,), in_specs=[row, row, vec, row], out_specs=[row, row, vec],
        out_shape=[jax.ShapeDtypeStruct((s, d), F32), jax.ShapeDtypeStruct((s, d), BF16),
                   jax.ShapeDtypeStruct((1, d), F32)],
        operands=[dh, x, w, dres], rider=rider)


def _rope_consts():
    lane = np.arange(LANES)
    in_head = lane % HEAD_DIM
    inv_freq = ROPE_THETA ** (-jnp.arange(0, ROT_DIM, 2, dtype=F32) / ROT_DIM)
    invf = jnp.where(jnp.asarray(in_head < ROT_DIM), jnp.tile(inv_freq, LANES // (ROT_DIM // 2)), 0.0)
    m_a = np.where(in_head < ROT_DIM // 2, -1.0, 0.0).astype(np.float32)
    m_b = np.where((in_head >= ROT_DIM // 2) & (in_head < ROT_DIM), 1.0, 0.0).astype(np.float32)
    block_diag = (lane[:, None] // HEAD_DIM == lane[None, :] // HEAD_DIM).astype(np.float32)
    return (invf.reshape(1, LANES).astype(F32), jnp.asarray(m_a).reshape(1, LANES),
            jnp.asarray(m_b).reshape(1, LANES), jnp.asarray(block_diag, dtype=BF16))


def _head_sums(v, bd):
    hi = v.astype(BF16)
    lo = (v - hi.astype(F32)).astype(BF16)
    return jnp.dot(hi, bd, preferred_element_type=F32) + jnp.dot(lo, bd, preferred_element_type=F32)


def _qk_fwd(proj, pos_col, qw2, kw2, consts, name, rider=None):
    s = proj.shape[0]
    width = 3 * N_SLOT_HEADS * HEAD_DIM
    tm = 128
    invf, m_a, m_b, bd = consts
    scale = HEAD_DIM ** -0.5

    def body(q_ref, k_ref, pos_ref, qw_ref, kw_ref, invf_ref, ma_ref, mb_ref, bd_ref, qo_ref, ko_ref):
        ang = pos_ref[...].astype(F32) * invf_ref[...]
        cos = jnp.cos(ang)
        sin = jnp.sin(ang)
        s_a = sin * ma_ref[...]
        s_b = sin * mb_ref[...]
        bdv = bd_ref[...]
        for src, w_ref, dst, sc in ((q_ref, qw_ref, qo_ref, scale), (k_ref, kw_ref, ko_ref, 1.0)):
            for cb in range(width // LANES):
                cols = slice(cb * LANES, (cb + 1) * LANES)
                t = src[:, cols]
                rstd = lax.rsqrt(_head_sums(t * t, bdv) * (1.0 / HEAD_DIM) + EPS)
                y = t * rstd * w_ref[...]
                r = y * cos + pltpu.roll(y, LANES - 8, axis=1) * s_a + pltpu.roll(y, 8, axis=1) * s_b
                dst[:, cols] = r * sc if sc != 1.0 else r

    vec = pl.BlockSpec((1, LANES), lambda i: (0, 0))
    return _pallas(
        body, name=name, grid=(s // tm,),
        in_specs=[pl.BlockSpec((tm, width), lambda i: (i, 0)), pl.BlockSpec((tm, width), lambda i: (i, 1)),
                  pl.BlockSpec((tm, 1), lambda i: (i, 0)), vec, vec, vec, vec, vec,
                  pl.BlockSpec((LANES, LANES), lambda i: (0, 0))],
        out_specs=[pl.BlockSpec((tm, width), lambda i: (i, 0))] * 2,
        out_shape=[jax.ShapeDtypeStruct((s, width), F32)] * 2,
        operands=[proj, proj, pos_col, qw2, kw2, invf, m_a, m_b, bd], rider=rider)


def _qk_bwd(dqn, dkn, dv, da, db, dgl, proj, pos_col, qw2, kw2, consts, name, rider=None):
    s = proj.shape[0]
    width = 3 * N_SLOT_HEADS * HEAD_DIM
    ch = da.shape[1]
    gate_w = dgl.shape[1]
    out_w = 3 * width + 2 * ch + gate_w
    assert out_w == proj.shape[1]
    tm = 128
    invf, m_a, m_b, bd = consts
    scale = HEAD_DIM ** -0.5

    def body(dq_ref, dk_ref, dv_ref, da_ref, db_ref, dgl_ref, q_ref, k_ref, pos_ref, qw_ref, kw_ref,
             invf_ref, ma_ref, mb_ref, bd_ref, out_ref, dqw_ref, dkw_ref):
        ang = pos_ref[...].astype(F32) * invf_ref[...]
        cos = jnp.cos(ang)
        sin = jnp.sin(ang)
        s_a = sin * ma_ref[...]
        s_b = sin * mb_ref[...]
        bdv = bd_ref[...]
        first = pl.program_id(0) == 0
        for src, dsrc, w_ref, col0, dw_ref, sc in ((q_ref, dq_ref, qw_ref, 0, dqw_ref, scale),
                                                   (k_ref, dk_ref, kw_ref, width, dkw_ref, 1.0)):
            dw_acc = jnp.zeros((1, LANES), F32)
            for cb in range(width // LANES):
                cols = slice(cb * LANES, (cb + 1) * LANES)
                t = src[:, cols]
                dr = dsrc[:, cols]
                if sc != 1.0:
                    dr = dr * sc
                dy = dr * cos + pltpu.roll(dr * s_a, 8, axis=1) + pltpu.roll(dr * s_b, LANES - 8, axis=1)
                rstd = lax.rsqrt(_head_sums(t * t, bdv) * (1.0 / HEAD_DIM) + EPS)
                xhat = t * rstd
                g = dy * w_ref[...]
                dt = rstd * (g - xhat * (_head_sums(g * xhat, bdv) * (1.0 / HEAD_DIM)))
                out_ref[:, col0 + cb * LANES: col0 + (cb + 1) * LANES] = dt.astype(BF16)
                dw_acc = dw_acc + jnp.sum(dy * xhat, axis=0, keepdims=True)
            dw_acc = dw_acc + pltpu.roll(dw_acc, HEAD_DIM, axis=1)

            @pl.when(first)
            def _(dw_ref=dw_ref, dw_acc=dw_acc):
                dw_ref[...] = dw_acc

            @pl.when(jnp.logical_not(first))
            def _(dw_ref=dw_ref, dw_acc=dw_acc):
                dw_ref[...] += dw_acc
        out_ref[:, 2 * width: 3 * width] = dv_ref[...].astype(BF16)
        out_ref[:, 3 * width: 3 * width + ch] = da_ref[...]
        out_ref[:, 3 * width + ch: 3 * width + 2 * ch] = db_ref[...]
        out_ref[:, 3 * width + 2 * ch: out_w] = dgl_ref[...]

    vec = pl.BlockSpec((1, LANES), lambda i: (0, 0))
    blk = lambda c: pl.BlockSpec((tm, width), lambda i: (i, c))
    cblk = pl.BlockSpec((tm, ch), lambda i: (i, 0))
    return _pallas(
        body, name=name, grid=(s // tm,),
        in_specs=[blk(0), blk(0), blk(0), cblk, cblk, pl.BlockSpec((tm, gate_w), lambda i: (i, 0)),
                  blk(0), blk(1), pl.BlockSpec((tm, 1), lambda i: (i, 0)), vec, vec, vec, vec, vec,
                  pl.BlockSpec((LANES, LANES), lambda i: (0, 0))],
        out_specs=[pl.BlockSpec((tm, out_w), lambda i: (i, 0)), vec, vec],
        out_shape=[jax.ShapeDtypeStruct((s, out_w), BF16)] + [jax.ShapeDtypeStruct((1, LANES), F32)] * 2,
        operands=[dqn, dkn, dv, da, db, dgl, proj, proj, pos_col, qw2, kw2, invf, m_a, m_b, bd],
        rider=rider)


def _row_chunks(n_rows, fn, chunk=256):
    def step(i, c):
        fn(pl.ds(pl.multiple_of(i * chunk, chunk), chunk))
        return c
    lax.fori_loop(0, n_rows // chunk, step, 0)


def _to_residue_major(dst, src, s, d, dst_off=0, cast=None):
    seq = s // d
    for r in range(d):
        v = src[...] if d == 1 else src[pl.ds(r, seq, stride=d), :]
        dst[dst_off + r * seq: dst_off + (r + 1) * seq, :] = v if cast is None else v.astype(cast)


def _from_residue_major(dst, src, s, d, src_off=0):
    seq = s // d
    for r in range(d):
        v = src[src_off + r * seq: src_off + (r + 1) * seq, :]
        if d == 1:
            dst[...] = v
        else:
            dst[pl.ds(r, seq, stride=d), :] = v


def _band_bias():
    qi = lax.broadcasted_iota(jnp.int32, (QBLK, KWIN), 0)
    kj = lax.broadcasted_iota(jnp.int32, (QBLK, KWIN), 1)
    return jnp.where(jnp.abs(kj - HALF_SPAN - qi) <= HALF_SPAN, 0.0, NEG_INF).astype(F32)


def _range_bias(base, seq):
    kj = lax.broadcasted_iota(jnp.int32, (1, KWIN), 1)
    lo = (base & -seq) - base + HALF_SPAN
    return jnp.where((kj >= lo) & (kj < lo + seq), 0.0, NEG_INF).astype(F32)


def _skewed_blocks(n_blk, produce, consume):
    produce(0, 0)
    for b in range(n_blk):
        consume(b, b % 2)
        if b + 1 < n_blk:
            produce(b + 1, (b + 1) % 2)


def _block_base(b):
    return b * QBLK if isinstance(b, int) else pl.multiple_of(b * QBLK, QBLK)


def _attn_fwd(qn, kn, proj, name, rider=None):
    s = qn.shape[0]
    n_pairs = N_SLOT_HEADS * HEAD_DIM // LANES
    v_col0 = 2 * qn.shape[1] // LANES
    nt_dims = (((1,), (1,)), ((), ()))

    def body(q_ref, k_ref, v_ref, attn_ref, lse_ref, attn_b_ref, q_rm, k_rm, v_rm, acc_rm, m_rm, l_rm,
             acc_p, m_p, l_p, m_run, l_run, acc_run, band, s_buf, m_buf):
        g = pl.program_id(1)
        zpad = jnp.zeros((HALF_SPAN, LANES), BF16)
        k_rm[0:HALF_SPAN, :] = zpad
        k_rm[s + HALF_SPAN: s + 2 * HALF_SPAN, :] = zpad
        v_rm[0:HALF_SPAN, 0:LANES] = zpad
        v_rm[s + HALF_SPAN: s + 2 * HALF_SPAN, 0:LANES] = zpad

        def ones_rows(rows):
            v_rm[pl.ds(rows.start, rows.size), LANES:2 * LANES] = jnp.ones((rows.size, LANES), BF16)

        _row_chunks(s + 2 * HALF_SPAN, ones_rows, chunk=2 * HALF_SPAN)
        band[...] = _band_bias()
        lane = lax.broadcasted_iota(jnp.int32, (QBLK, LANES), 1)
        low = lane < HEAD_DIM
        n_blk = s // QBLK

        for gi, d in enumerate(DILATIONS):
            @pl.when(g == gi)
            def _(gi=gi, d=d):
                seq = s // d
                _to_residue_major(q_rm, q_ref, s, d, cast=BF16)
                _to_residue_major(k_rm, k_ref, s, d, dst_off=HALF_SPAN, cast=BF16)
                _to_residue_major(v_rm.at[:, 0:LANES], v_ref, s, d, dst_off=HALF_SPAN, cast=BF16)

                def scores(b, slot):
                    base = _block_base(b)
                    q = q_rm[pl.ds(base, QBLK), :]
                    zero = jnp.zeros_like(q)
                    q2 = jnp.concatenate([jnp.where(low, q, zero), jnp.where(low, zero, q)], axis=0)
                    sc = lax.dot_general(q2, k_rm[pl.ds(base, KWIN), :], nt_dims, preferred_element_type=F32)
                    bias = band[...] + _range_bias(base, seq)
                    for hh in range(2):
                        rows = slice(hh * QBLK, (hh + 1) * QBLK)
                        sh = sc[rows, :] + bias
                        s_buf[slot, rows, :] = sh
                        m_buf[slot, rows, :] = jnp.broadcast_to(jnp.max(sh, axis=-1, keepdims=True), (QBLK, LANES))

                def outputs(b, slot):
                    base = _block_base(b)
                    sv = s_buf[slot]
                    mb = m_buf[slot]
                    p = jnp.exp(jnp.concatenate([sv[:, 0:LANES] - mb, sv[:, LANES:2 * LANES] - mb], axis=1))
                    pv = jnp.dot(p.astype(BF16), v_rm[pl.ds(base, KWIN), :], preferred_element_type=F32)
                    rows = pl.ds(base, QBLK)
                    acc_rm[rows, :] = jnp.where(low, pv[0:QBLK, 0:LANES], pv[QBLK:2 * QBLK, 0:LANES])
                    l_rm[rows, :] = jnp.where(low, pv[0:QBLK, LANES:2 * LANES], pv[QBLK:2 * QBLK, LANES:2 * LANES])
                    m_rm[rows, :] = jnp.where(low, mb[0:QBLK, :], mb[QBLK:2 * QBLK, :])

                _skewed_blocks(n_blk, scores, outputs)
                if d == 1:
                    src = (acc_rm, m_rm, l_rm)
                else:
                    for dst_, src_ in ((acc_p, acc_rm), (m_p, m_rm), (l_p, l_rm)):
                        _from_residue_major(dst_, src_, s, d)
                    src = (acc_p, m_p, l_p)

                def combine(rows):
                    a_g, m_g, l_g = src[0][rows, :], src[1][rows, :], src[2][rows, :]
                    if gi == 0:
                        m_new, l_new, a_new = m_g, l_g, a_g
                    else:
                        m_old = m_run[rows, :]
                        m_new = jnp.maximum(m_old, m_g)
                        w_old = jnp.exp(m_old - m_new)
                        w_g = jnp.exp(m_g - m_new)
                        l_new = l_run[rows, :] * w_old + l_g * w_g
                        a_new = acc_run[rows, :] * w_old + a_g * w_g
                    if gi == len(DILATIONS) - 1:
                        out = a_new / l_new
                        attn_ref[rows, :] = out
                        attn_b_ref[rows, :] = out.astype(BF16)
                        lse_ref[rows, :] = m_new + jnp.log(l_new)
                    else:
                        m_run[rows, :] = m_new
                        l_run[rows, :] = l_new
                        acc_run[rows, :] = a_new

                _row_chunks(s, combine)

    qk_spec = pl.BlockSpec((s, LANES), lambda hp, g: (0, g * n_pairs + hp))
    v_spec = pl.BlockSpec((s, LANES), lambda hp, g: (0, v_col0 + g * n_pairs + hp))
    o_spec = pl.BlockSpec((s, LANES), lambda hp, g: (0, hp))
    f32buf = pltpu.VMEM((s, LANES), F32)
    return _pallas(
        body, name=name, grid=(n_pairs, len(DILATIONS)), in_specs=[qk_spec, qk_spec, v_spec],
        out_specs=[o_spec, o_spec, o_spec],
        out_shape=[jax.ShapeDtypeStruct((s, n_pairs * LANES), F32)] * 2
        + [jax.ShapeDtypeStruct((s, n_pairs * LANES), BF16)],
        operands=[qn, kn, proj],
        scratch_shapes=[pltpu.VMEM((s, LANES), BF16), pltpu.VMEM((s + 2 * HALF_SPAN, LANES), BF16),
                        pltpu.VMEM((s + 2 * HALF_SPAN, 2 * LANES), BF16)] + [f32buf] * 9
        + [pltpu.VMEM((QBLK, KWIN), F32), pltpu.VMEM((2, 2 * QBLK, KWIN), F32),
           pltpu.VMEM((2, 2 * QBLK, LANES), F32)],
        rider=rider)


def _attn_bwd(qn, kn, proj, dattn, attn, lse, bd, name, rider=None):
    s = qn.shape[0]
    n_pairs = N_SLOT_HEADS * HEAD_DIM // LANES
    v_col0 = 2 * qn.shape[1] // LANES
    nt_dims = (((1,), (1,)), ((), ()))
    tn_dims = (((0,), (0,)), ((), ()))
    spad = s + 2 * HALF_SPAN

    def body(q_ref, k_ref, v_ref, do_ref, o_ref, lse_ref, bd_ref, dq_ref, dk_ref, dv_ref,
             q_rm, k_rm, v_rm, do_rm, lse0_rm, lse1_rm, dd0_rm, dd1_rm, dq_rm, dk_rm, dv_rm,
             lse0_p, lse1_p, dd0_p, dd1_p, band, p_buf, ds_buf):
        g = pl.program_id(1)
        zpad = jnp.zeros((HALF_SPAN, LANES), BF16)
        for buf in (k_rm, v_rm):
            buf[0:HALF_SPAN, :] = zpad
            buf[s + HALF_SPAN: spad, :] = zpad
        zf = jnp.zeros((HALF_SPAN, LANES), F32)
        for buf in (dk_rm, dv_rm):
            buf[0:HALF_SPAN, :] = zf
            buf[s + HALF_SPAN: spad, :] = zf
        band[...] = _band_bias()

        def clear(rows):
            z = jnp.zeros((rows.size, LANES), F32)
            dk_rm[pl.ds(rows.start + HALF_SPAN, rows.size), :] = z
            dv_rm[pl.ds(rows.start + HALF_SPAN, rows.size), :] = z

        _row_chunks(s, clear)

        def prepare(rows):
            lo = lax.broadcasted_iota(jnp.int32, (rows.size, LANES), 1) < HEAD_DIM
            dsum = _head_sums(do_ref[rows, :] * o_ref[rows, :], bd_ref[...])
            dswap = pltpu.roll(dsum, HEAD_DIM, axis=1)
            dd0_p[rows, :] = jnp.where(lo, dsum, dswap)
            dd1_p[rows, :] = jnp.where(lo, dswap, dsum)
            lv = lse_ref[rows, :]
            lswap = pltpu.roll(lv, HEAD_DIM, axis=1)
            lse0_p[rows, :] = jnp.where(lo, lv, lswap)
            lse1_p[rows, :] = jnp.where(lo, lswap, lv)

        @pl.when(g == 0)
        def _():
            _row_chunks(s, prepare)
        lane = lax.broadcasted_iota(jnp.int32, (QBLK, LANES), 1)
        low = lane < HEAD_DIM
        n_blk = s // QBLK

        def stacked(ref, rows):
            val = ref[rows, :]
            zero = jnp.zeros_like(val)
            return jnp.concatenate([jnp.where(low, val, zero), jnp.where(low, zero, val)], axis=0)

        for gi, d in enumerate(DILATIONS):
            @pl.when(g == gi)
            def _(d=d):
                seq = s // d
                _to_residue_major(q_rm, q_ref, s, d, cast=BF16)
                _to_residue_major(k_rm, k_ref, s, d, dst_off=HALF_SPAN, cast=BF16)
                _to_residue_major(v_rm, v_ref, s, d, dst_off=HALF_SPAN, cast=BF16)
                _to_residue_major(do_rm, do_ref, s, d, cast=BF16)
                for dst_, src_ in ((lse0_rm, lse0_p), (lse1_rm, lse1_p), (dd0_rm, dd0_p), (dd1_rm, dd1_p)):
                    _to_residue_major(dst_, src_, s, d)

                def scores(b, slot):
                    base = _block_base(b)
                    rows = pl.ds(base, QBLK)
                    win = pl.ds(base, KWIN)
                    sc = lax.dot_general(stacked(q_rm, rows), k_rm[win, :], nt_dims, preferred_element_type=F32)
                    dp = lax.dot_general(stacked(do_rm, rows), v_rm[win, :], nt_dims, preferred_element_type=F32)
                    bias = band[...] + _range_bias(base, seq)
                    for hh, (lse_r, dd_r) in enumerate(((lse0_rm, dd0_rm), (lse1_rm, dd1_rm))):
                        r = slice(hh * QBLK, (hh + 1) * QBLK)
                        lse_h = lse_r[rows, :]
                        dd_h = dd_r[rows, :]
                        sh = sc[r, :] + bias
                        p = jnp.exp(jnp.concatenate([sh[:, 0:LANES] - lse_h, sh[:, LANES:KWIN] - lse_h], axis=1))
                        dph = dp[r, :]
                        ds = p * jnp.concatenate([dph[:, 0:LANES] - dd_h, dph[:, LANES:KWIN] - dd_h], axis=1)
                        p_buf[slot, r, :] = p.astype(BF16)
                        ds_buf[slot, r, :] = ds.astype(BF16)

                def grads(b, slot):
                    base = _block_base(b)
                    rows = pl.ds(base, QBLK)
                    win = pl.ds(base, KWIN)
                    p = p_buf[slot]
                    ds = ds_buf[slot]
                    dq2 = jnp.dot(ds, k_rm[win, :], preferred_element_type=F32)
                    dq_rm[rows, :] = jnp.where(low, dq2[0:QBLK, :], dq2[QBLK:2 * QBLK, :])
                    dk_rm[win, :] += lax.dot_general(ds, stacked(q_rm, rows), tn_dims, preferred_element_type=F32)
                    dv_rm[win, :] += lax.dot_general(p, stacked(do_rm, rows), tn_dims, preferred_element_type=F32)

                _skewed_blocks(n_blk, scores, grads)
                _from_residue_major(dq_ref, dq_rm, s, d)
                _from_residue_major(dk_ref, dk_rm, s, d, src_off=HALF_SPAN)
                _from_residue_major(dv_ref, dv_rm, s, d, src_off=HALF_SPAN)

    qk_spec = pl.BlockSpec((s, LANES), lambda hp, g: (0, g * n_pairs + hp))
    v_spec = pl.BlockSpec((s, LANES), lambda hp, g: (0, v_col0 + g * n_pairs + hp))
    o_spec = pl.BlockSpec((s, LANES), lambda hp, g: (0, hp))
    width = qn.shape[1]
    f32buf = pltpu.VMEM((s, LANES), F32)
    f32pad = pltpu.VMEM((spad, LANES), F32)
    return _pallas(
        body, name=name, grid=(n_pairs, len(DILATIONS)),
        in_specs=[qk_spec, qk_spec, v_spec, o_spec, o_spec, o_spec,
                  pl.BlockSpec((LANES, LANES), lambda hp, g: (0, 0))],
        out_specs=[qk_spec, qk_spec, qk_spec],
        out_shape=[jax.ShapeDtypeStruct((s, width), F32)] * 3,
        operands=[qn, kn, proj, dattn, attn, lse, bd],
        scratch_shapes=[pltpu.VMEM((s, LANES), BF16), pltpu.VMEM((spad, LANES), BF16),
                        pltpu.VMEM((spad, LANES), BF16), pltpu.VMEM((s, LANES), BF16),
                        f32buf, f32buf, f32buf, f32buf, f32buf, f32pad, f32pad,
                        f32buf, f32buf, f32buf, f32buf, pltpu.VMEM((QBLK, KWIN), F32),
                        pltpu.VMEM((2, 2 * QBLK, KWIN), BF16), pltpu.VMEM((2, 2 * QBLK, KWIN), BF16)],
        rider=rider)


CONV_PAD = 16


def _conv_fwd(proj, conv_w, conv_b, col0, name, rider=None):
    s = proj.shape[0]
    ch = conv_w.shape[1]
    nblk = ch // LANES
    a0 = col0 // LANES
    tr = 256
    shift = CONV_PAD - (CONV_WIDTH - 1) // 2

    def body(a_ref, b_ref, w_ref, bias_ref, u0_ref, uc_ref, pad):
        z = jnp.zeros((CONV_PAD, LANES), F32)
        pad[0:CONV_PAD, :] = z
        pad[s + CONV_PAD: s + 2 * CONV_PAD, :] = z

        def glu(rows):
            u0 = a_ref[rows, :] * jax.nn.sigmoid(b_ref[rows, :])
            u0_ref[rows, :] = u0
            pad[pl.ds(rows.start + CONV_PAD, rows.size), :] = u0

        _row_chunks(s, glu)
        for t in range(0, s, tr):
            acc = jnp.broadcast_to(bias_ref[...], (tr, LANES))
            for k in range(CONV_WIDTH):
                acc = acc + w_ref[k:k + 1, :] * pad[t + k + shift: t + k + shift + tr, :]
            uc_ref[t:t + tr, :] = acc

    return _pallas(
        body, name=name, grid=(nblk,),
        in_specs=[pl.BlockSpec((s, LANES), lambda c: (0, a0 + c)),
                  pl.BlockSpec((s, LANES), lambda c: (0, a0 + nblk + c)),
                  pl.BlockSpec((CONV_WIDTH, LANES), lambda c: (0, c)),
                  pl.BlockSpec((1, LANES), lambda c: (0, c))],
        out_specs=[pl.BlockSpec((s, LANES), lambda c: (0, c))] * 2,
        out_shape=[jax.ShapeDtypeStruct((s, ch), F32)] * 2, operands=[proj, proj, conv_w, conv_b],
        scratch_shapes=[pltpu.VMEM((s + 2 * CONV_PAD, LANES), F32)], rider=rider)


def _ln_silu_fwd(uc, ln_w, ln_b, name):
    s, ch = uc.shape
    tm = 256

    def body(u_ref, w_ref, b_ref, o_ref):
        u = u_ref[...]
        mu = jnp.mean(u, axis=-1, keepdims=True)
        xc = u - mu
        rstd = lax.rsqrt(jnp.mean(xc * xc, axis=-1, keepdims=True) + EPS)
        z = xc * rstd * w_ref[...] + b_ref[...]
        o_ref[...] = (z * jax.nn.sigmoid(z)).astype(BF16)

    row = pl.BlockSpec((tm, ch), lambda i: (i, 0))
    vec = pl.BlockSpec((1, ch), lambda i: (0, 0))
    return pl.pallas_call(
        body, name=name, grid=(s // tm,), in_specs=[row, vec, vec], out_specs=row,
        out_shape=jax.ShapeDtypeStruct((s, ch), BF16), compiler_params=_params(),
    )(uc, ln_w, ln_b)


def _ln_silu_bwd(du3, uc, ln_w, ln_b, name):
    s, ch = uc.shape
    tm = 256

    def body(d_ref, u_ref, w_ref, b_ref, du_ref, dw_ref, db_ref):
        u = u_ref[...]
        mu = jnp.mean(u, axis=-1, keepdims=True)
        xc = u - mu
        rstd = lax.rsqrt(jnp.mean(xc * xc, axis=-1, keepdims=True) + EPS)
        xhat = xc * rstd
        z = xhat * w_ref[...] + b_ref[...]
        sg = jax.nn.sigmoid(z)
        dz = d_ref[...] * (sg * (1.0 + z * (1.0 - sg)))
        dxh = dz * w_ref[...]
        du_ref[...] = rstd * (dxh - jnp.mean(dxh, axis=-1, keepdims=True)
                              - xhat * jnp.mean(dxh * xhat, axis=-1, keepdims=True))
        pw = jnp.sum(dz * xhat, axis=0, keepdims=True)
        pb = jnp.sum(dz, axis=0, keepdims=True)
        first = pl.program_id(0) == 0

        @pl.when(first)
        def _():
            dw_ref[...] = pw
            db_ref[...] = pb

        @pl.when(jnp.logical_not(first))
        def _():
            dw_ref[...] += pw
            db_ref[...] += pb

    row = pl.BlockSpec((tm, ch), lambda i: (i, 0))
    vec = pl.BlockSpec((1, ch), lambda i: (0, 0))
    return pl.pallas_call(
        body, name=name, grid=(s // tm,), in_specs=[row, row, vec, vec], out_specs=[row, vec, vec],
        out_shape=[jax.ShapeDtypeStruct((s, ch), F32), jax.ShapeDtypeStruct((1, ch), F32),
                   jax.ShapeDtypeStruct((1, ch), F32)],
        compiler_params=_params(),
    )(du3, uc, ln_w, ln_b)


def _conv_bwd(duc, u0, proj, conv_w, col0, name, rider=None):
    s = proj.shape[0]
    ch = conv_w.shape[1]
    nblk = ch // LANES
    a0 = col0 // LANES
    tr = 256
    half = (CONV_WIDTH - 1) // 2
    shift = CONV_PAD - half

    def body(duc_ref, u0_ref, a_ref, b_ref, w_ref, da_ref, db_ref, dw_ref, dbias_ref, pad_d, pad_u):
        z = jnp.zeros((CONV_PAD, LANES), F32)
        for buf in (pad_d, pad_u):
            buf[0:CONV_PAD, :] = z
            buf[s + CONV_PAD: s + 2 * CONV_PAD, :] = z

        def fill(rows):
            dst = pl.ds(rows.start + CONV_PAD, rows.size)
            pad_d[dst, :] = duc_ref[rows, :]
            pad_u[dst, :] = u0_ref[rows, :]

        _row_chunks(s, fill)
        dw_acc = [jnp.zeros((8, LANES), F32) for _ in range(CONV_WIDTH)]
        dbias_acc = jnp.zeros((8, LANES), F32)
        for t in range(0, s, tr):
            d_t = duc_ref[t:t + tr, :]
            dbias_acc = dbias_acc + jnp.sum(d_t.reshape(tr // 8, 8, LANES), axis=0)
            du0 = jnp.zeros((tr, LANES), F32)
            for k in range(CONV_WIDTH):
                du0 = du0 + w_ref[k:k + 1, :] * pad_d[t - k + half + CONV_PAD: t - k + half + CONV_PAD + tr, :]
                prod = d_t * pad_u[t + k + shift: t + k + shift + tr, :]
                dw_acc[k] = dw_acc[k] + jnp.sum(prod.reshape(tr // 8, 8, LANES), axis=0)
            av = a_ref[t:t + tr, :]
            sg = jax.nn.sigmoid(b_ref[t:t + tr, :])
            da_ref[t:t + tr, :] = (du0 * sg).astype(BF16)
            db_ref[t:t + tr, :] = (du0 * av * sg * (1.0 - sg)).astype(BF16)
        for k in range(CONV_WIDTH):
            dw_ref[k:k + 1, :] = jnp.sum(dw_acc[k], axis=0, keepdims=True)
        dbias_ref[...] = jnp.sum(dbias_acc, axis=0, keepdims=True)

    col = lambda off: pl.BlockSpec((s, LANES), lambda c: (0, off + c))
    return _pallas(
        body, name=name, grid=(nblk,),
        in_specs=[col(0), col(0), col(a0), col(a0 + nblk),
                  pl.BlockSpec((CONV_WIDTH, LANES), lambda c: (0, c))],
        out_specs=[col(0), col(0), pl.BlockSpec((CONV_WIDTH, LANES), lambda c: (0, c)),
                   pl.BlockSpec((1, LANES), lambda c: (0, c))],
        out_shape=[jax.ShapeDtypeStruct((s, ch), BF16)] * 2
        + [jax.ShapeDtypeStruct((CONV_WIDTH, ch), F32), jax.ShapeDtypeStruct((1, ch), F32)],
        operands=[duc, u0, proj, proj, conv_w],
        scratch_shapes=[pltpu.VMEM((s + 2 * CONV_PAD, LANES), F32)] * 2, rider=rider)


GATE_BLK = 512


def _gate_fwd(proj, bg, y_a, y_b, col0, name):
    s, d = y_a.shape
    tm = 256
    g0 = col0 // GATE_BLK
    nb = d // GATE_BLK

    def body(ga_ref, gb_ref, ba_ref, bb_ref, ya_ref, yb_ref, o_ref):
        g_a = jax.nn.sigmoid(ga_ref[...] + ba_ref[...])
        g_b = jax.nn.sigmoid(gb_ref[...] + bb_ref[...])
        o_ref[...] = (g_a * ya_ref[...] + g_b * yb_ref[...]).astype(BF16)

    act = pl.BlockSpec((tm, GATE_BLK), lambda i, j: (i, j))
    return pl.pallas_call(
        body, name=name, grid=(s // tm, nb),
        in_specs=[pl.BlockSpec((tm, GATE_BLK), lambda i, j: (i, g0 + j)),
                  pl.BlockSpec((tm, GATE_BLK), lambda i, j: (i, g0 + nb + j)),
                  pl.BlockSpec((None, 1, GATE_BLK), lambda i, j: (0, 0, j)),
                  pl.BlockSpec((None, 1, GATE_BLK), lambda i, j: (1, 0, j)), act, act],
        out_specs=act, out_shape=jax.ShapeDtypeStruct((s, d), BF16), compiler_params=_params(),
    )(proj, proj, bg, bg, y_a, y_b)


def _out_proj_bwd_gates(dx1, w_out, proj, bg, y_a, y_b, col0, name, after=()):
    s, d = y_a.shape
    tm = 256
    half = d // 2
    assert col0 % half == 0
    c0 = col0 // half
    nt_dims = (((1,), (1,)), ((), ()))

    def body(dx_ref, w_ref, a0_ref, a1_ref, b0_ref, b1_ref, bias_ref, ya_ref, yb_ref,
             dgl_ref, dya_ref, dyb_ref, db_ref):
        dm = lax.dot_general(dx_ref[...], w_ref[...], nt_dims, preferred_element_type=F32)
        parts = []
        for br, (lo_ref, hi_ref, y_ref, dy_ref) in enumerate(((a0_ref, a1_ref, ya_ref, dya_ref),
                                                              (b0_ref, b1_ref, yb_ref, dyb_ref))):
            logits = jnp.concatenate([lo_ref[...], hi_ref[...]], axis=1)
            gate = jax.nn.sigmoid(logits + bias_ref[br])
            dy_ref[...] = (dm * gate).astype(BF16)
            dgl = dm * y_ref[...] * gate * (1.0 - gate)
            dgl_ref[:, br * d:(br + 1) * d] = dgl.astype(BF16)
            parts.append(jnp.sum(dgl, axis=0, keepdims=True))
        part = jnp.concatenate(parts, axis=0)
        first = pl.program_id(0) == 0

        @pl.when(first)
        def _():
            db_ref[...] = part

        @pl.when(jnp.logical_not(first))
        def _():
            db_ref[...] += part

    row = pl.BlockSpec((tm, d), lambda i: (i, 0))
    logit_blk = lambda k: pl.BlockSpec((tm, half), functools.partial(lambda i, k: (i, c0 + k), k=k))
    return _pallas(
        body, name=name, grid=(s // tm,),
        in_specs=[row, pl.BlockSpec((d, d), lambda i: (0, 0)), logit_blk(0), logit_blk(1), logit_blk(2),
                  logit_blk(3), pl.BlockSpec((2, 1, d), lambda i: (0, 0, 0)), row, row],
        out_specs=[pl.BlockSpec((tm, 2 * d), lambda i: (i, 0)), row, row, pl.BlockSpec((2, d), lambda i: (0, 0))],
        out_shape=[jax.ShapeDtypeStruct((s, 2 * d), BF16), jax.ShapeDtypeStruct((s, d), BF16),
                   jax.ShapeDtypeStruct((s, d), BF16), jax.ShapeDtypeStruct((2, d), F32)],
        operands=[dx1, w_out, proj, proj, proj, proj, bg, y_a, y_b], after=after)


def _ffn_in_swiglu(h2, w_blocked, name):
    s, k = h2.shape
    nblk, _, tn = w_blocked.shape
    ff = nblk // 2 * tn
    tm = 512

    def body(a_ref, wg_ref, wu_ref, g_ref, u_ref, act_ref):
        a = a_ref[...]
        gt = jnp.dot(a, wg_ref[...], preferred_element_type=F32)
        up = jnp.dot(a, wu_ref[...], preferred_element_type=F32)
        g_ref[...] = gt
        u_ref[...] = up
        act_ref[...] = (gt * jax.nn.sigmoid(gt) * up).astype(BF16)

    out = pl.BlockSpec((tm, tn), lambda j, i: (i, j))
    return pl.pallas_call(
        body, name=name, grid=(nblk // 2, s // tm),
        in_specs=[pl.BlockSpec((tm, k), lambda j, i: (i, 0)),
                  pl.BlockSpec((None, k, tn), lambda j, i: (j, 0, 0)),
                  pl.BlockSpec((None, k, tn), lambda j, i: (nblk // 2 + j, 0, 0))],
        out_specs=[out, out, out],
        out_shape=[jax.ShapeDtypeStruct((s, ff), F32), jax.ShapeDtypeStruct((s, ff), F32),
                   jax.ShapeDtypeStruct((s, ff), BF16)],
        compiler_params=_params(),
    )(h2, w_blocked, w_blocked)


def _ffn_out_bwd_swiglu(dy, w_ffn_out, gate, up, name, rider=None):
    s, d = dy.shape
    ff = gate.shape[1]
    tm = 256
    nt_dims = (((1,), (1,)), ((), ()))

    def body(dy_ref, w_ref, g_ref, u_ref, o_ref):
        dv = lax.dot_general(dy_ref[...], w_ref[...], nt_dims, preferred_element_type=F32)
        gt = g_ref[...]
        sg = jax.nn.sigmoid(gt)
        o_ref[:, 0:ff] = (dv * u_ref[...] * (sg * (1.0 + gt * (1.0 - sg)))).astype(BF16)
        o_ref[:, ff:2 * ff] = (dv * gt * sg).astype(BF16)

    row = pl.BlockSpec((tm, ff), lambda i: (i, 0))
    return _pallas(
        body, name=name, grid=(s // tm,),
        in_specs=[pl.BlockSpec((tm, d), lambda i: (i, 0)), pl.BlockSpec((ff, d), lambda i: (0, 0)), row, row],
        out_specs=pl.BlockSpec((tm, 2 * ff), lambda i: (i, 0)),
        out_shape=jax.ShapeDtypeStruct((s, 2 * ff), BF16), operands=[dy, w_ffn_out, gate, up], rider=rider)


def _out_proj_rmsnorm(mixed, w_out, x, norm_w, name, rider=None):
    s, k = mixed.shape
    d = w_out.shape[1]
    tm = 512

    def body(a_ref, w_ref, x_ref, nw_ref, x1_ref, h2_ref):
        x1 = x_ref[...] + jnp.dot(a_ref[...], w_ref[...], preferred_element_type=F32)
        x1_ref[...] = x1
        rstd = lax.rsqrt(jnp.mean(x1 * x1, axis=-1, keepdims=True) + EPS)
        h2_ref[...] = (x1 * rstd * nw_ref[...]).astype(BF16)

    row = pl.BlockSpec((tm, d), lambda i: (i, 0))
    return _pallas(
        body, name=name, grid=(s // tm,),
        in_specs=[pl.BlockSpec((tm, k), lambda i: (i, 0)), pl.BlockSpec((k, d), lambda i: (0, 0)), row,
                  pl.BlockSpec((1, d), lambda i: (0, 0))],
        out_specs=[row, row],
        out_shape=[jax.ShapeDtypeStruct((s, d), F32), jax.ShapeDtypeStruct((s, d), BF16)],
        operands=[mixed, w_out, x, norm_w], rider=rider)


def _ffn_out_loss(act, w_ffn_out, x1, target, name):
    s, k = act.shape
    d = w_ffn_out.shape[1]
    tm = 512

    def body(a_ref, w_ref, x1_ref, t_ref, dy_ref, dyb_ref, loss_ref, acc):
        y = x1_ref[...] + jnp.dot(a_ref[...], w_ref[...], preferred_element_type=F32)
        diff = y - t_ref[...]
        dy = diff * (1.0 / d)
        dy_ref[...] = dy
        dyb_ref[...] = dy.astype(BF16)
        part = jnp.sum((diff * diff).reshape(tm // 8, 8, d), axis=0)
        i = pl.program_id(0)

        @pl.when(i == 0)
        def _():
            acc[...] = part

        @pl.when(i > 0)
        def _():
            acc[...] += part

        @pl.when(i == pl.num_programs(0) - 1)
        def _():
            loss_ref[...] = (0.5 / d) * jnp.sum(jnp.sum(acc[...], axis=1, keepdims=True), axis=0, keepdims=True)

    row = pl.BlockSpec((tm, d), lambda i: (i, 0))
    return pl.pallas_call(
        body, name=name, grid=(s // tm,),
        in_specs=[pl.BlockSpec((tm, k), lambda i: (i, 0)), pl.BlockSpec((k, d), lambda i: (0, 0)), row, row],
        out_specs=[row, row, pl.BlockSpec((1, 1), lambda i: (0, 0))],
        out_shape=[jax.ShapeDtypeStruct((s, d), F32), jax.ShapeDtypeStruct((s, d), BF16),
                   jax.ShapeDtypeStruct((1, 1), F32)],
        scratch_shapes=[pltpu.VMEM((8, d), F32)], compiler_params=_params(),
    )(act, w_ffn_out, x1, target)


LATE_GATHER = ("w_o_attn", "w_pw_conv", "w_out", "w_ffn_in", "w_ffn_out")
EARLY_REDUCE = LATE_GATHER


def _blocks_by_half(g):
    if g.ndim == 2:
        g = g.reshape(N_CHIPS, g.shape[0] // N_CHIPS, g.shape[1])
    return g.reshape(N_CHIPS, 2, g.shape[1] // 2, g.shape[2])


def _forward_backward(x, pos_col, target, wts, first_gather, late_bufs, pos_arr):
    wts = dict(wts)
    consts = _rope_consts()
    bd = consts[3]
    qw2 = jnp.tile(wts["q_norm_w"], (1, LANES // HEAD_DIM))
    kw2 = jnp.tile(wts["k_norm_w"], (1, LANES // HEAD_DIM))
    qkv_w = 3 * N_SLOT_HEADS * HEAD_DIM
    conv_col0 = 3 * qkv_w

    h = _rmsnorm_fwd(x, wts["norm1_w"], "rms1_fwd")
    slot_order = jnp.bitwise_xor(pos_arr[1], jnp.asarray([0, 2, 1, 3], jnp.int32))
    blocked = lambda buf: buf.reshape(N_CHIPS, -1, buf.shape[3])
    near = first_gather
    proj = _proj_by_slot(h, blocked(near[2][0]), slot_order, 0, 1, "mm_proj_own", after=list(late_bufs))
    near = _split_middle(near, after=[proj], name="allgather_w_in_near_forward")
    far, _ = _split_start(_gather_both_legs_rider(near[2][:1], near[2][1:], FAR_CHIPS), "allgather_w_in_far_start")
    _, bufs = _split_wait((near[0], near[1], far[2], near[3]), after=[], name="allgather_w_in_near_wait")
    proj = _proj_by_slot(h, blocked(bufs[0]), slot_order, 1, len(NEAR_CHIPS), "mm_proj_near", proj=proj)
    far = _split_middle((far[0], far[1], bufs, far[3]), after=[proj], name="allgather_w_in_far_forward")
    (w_in_buf, conv_w_buf, b_gate_buf), _ = _split_wait(far, after=[], name="allgather_w_in_far_wait")
    wts["w_in"] = w_in_buf.reshape(N_CHIPS, -1, w_in_buf.shape[3])
    wts["conv_w"] = conv_w_buf.transpose(1, 0, 2).reshape(CONV_WIDTH, -1)
    wts["b_gate"] = b_gate_buf.transpose(1, 0, 2).reshape(2, 1, -1)
    ch = wts["conv_w"].shape[1]
    gate_col0 = conv_col0 + 2 * ch
    n_mix = LATE_GATHER.index("w_ffn_in")
    mix_gather, started = _split_start(_gather_ici_rider(late_bufs[:n_mix], []), "late_gather_mix_start",
                                       after=[wts["w_in"]])
    ffn_gather, started = _split_start(_gather_ici_rider(late_bufs[n_mix:], []), "late_gather_ffn_start",
                                       after=[started])
    proj = _proj_by_slot(h, wts["w_in"], slot_order, 1 + len(NEAR_CHIPS), len(FAR_CHIPS), "mm_proj_far", proj=proj,
                         after=[started])
    qn, kn = _qk_fwd(proj, pos_col, qw2, kw2, consts, "qk_fwd")
    attn, lse, attn_b = _attn_fwd(qn, kn, proj, "attn_fwd")

    def gathered(names, bufs):
        for n, buf in zip(names, bufs):
            full = buf.reshape(N_CHIPS, -1, buf.shape[3])
            wts[n] = full.reshape(-1, full.shape[2]) if n in ROW_SHARDED else full

    mix_bufs, _ = _split_wait(mix_gather, after=[attn_b], name="late_gather_mix_wait")
    (u0, uc), mix_bufs = _conv_fwd(proj, wts["conv_w"], wts["conv_b"], conv_col0, "conv_fwd",
                                   rider=_gather_forward_rider(mix_bufs))
    gathered(LATE_GATHER[:n_mix], mix_bufs)
    y_a = _matmul(attn_b, wts["w_o_attn"], mode="nn", tm=1024, tn=256, tk=512, out_dtype=F32, name="mm_ya",
                  b_blocked=True)
    u3 = _ln_silu_fwd(uc, wts["conv_ln_w"], wts["conv_ln_b"], "ln_fwd")
    y_b = _matmul(u3, wts["w_pw_conv"], mode="nn", tm=1024, tn=256, tk=512, out_dtype=F32, name="mm_yb",
                  b_blocked=True)
    mixed = _gate_fwd(proj, wts["b_gate"], y_a, y_b, gate_col0, "gate_fwd")
    ffn_bufs, _ = _split_wait(ffn_gather, after=[mixed], name="late_gather_ffn_wait")
    (x1, h2), ffn_bufs = _out_proj_rmsnorm(mixed, wts["w_out"], x, wts["norm2_w"], "mm_x1_rms2",
                                           rider=_gather_forward_rider(ffn_bufs))
    gathered(LATE_GATHER[n_mix:], ffn_bufs)
    gate, up, act = _ffn_in_swiglu(h2, wts["w_ffn_in"], "mm_gu_swiglu")
    dy, dy_b16, loss = _ffn_out_loss(act, wts["w_ffn_out"], x1, target, "mm_x2_loss")

    g = {}
    by_chip = {}

    def pair_add(n, blocks, received):
        return _add_own_half(blocks, received, pos_arr, f"grads_pair_add_{n}")

    g_ffn_out = _blocks_by_half(
        _matmul(act, dy_b16, mode="tn", tm=1408, tn=1024, tk=2048, out_dtype=F32, name="mm_dwffnout"))
    dgu, (received,) = _ffn_out_bwd_swiglu(dy_b16, wts["w_ffn_out"], gate, up, "mm_dact_swiglu_bwd",
                                           rider=_pair_exchange_rider([g_ffn_out], halved=True))
    to_send, own = pair_add("w_ffn_out", g_ffn_out, received)
    dh2, (by_chip["w_ffn_out"],) = _matmul(
        dgu, wts["w_ffn_in"], mode="nt", tm=1024, tn=1024, tk=1408, out_dtype=F32, name="mm_dh2", b_blocked=True,
        rider=_chip_exchange_rider([to_send], [own]))
    g_ffn_in = _blocks_by_half(_matmul(h2, dgu, mode="tn", tm=512, tn=1408, tk=2048, out_dtype=F32,
                                       name="mm_dwffnin", out_blocked=N_CHIPS, cols_outer=True))
    exchanging, started = _split_start(_pair_exchange_rider([g_ffn_in], halved=True), "grads_ffn_in_pair_start")
    dx1, dx1_b16, g["norm2_w"] = _rmsnorm_bwd(dh2, x1, wts["norm2_w"], dy, "rms2_bwd")
    g["w_out"] = _matmul(mixed, dx1_b16, mode="tn", tm=512, tn=1024, tk=2048, out_dtype=F32, name="mm_dwout",
                         after=[started])
    dgl, dy_a, dy_b, g["b_gate"] = _out_proj_bwd_gates(dx1_b16, wts["w_out"], proj, wts["b_gate"], y_a, y_b,
                                                       gate_col0, "mm_dmixed_gate_bwd")
    (received,), (g_ffn_in,) = _split_wait(exchanging, after=[dgl], name="grads_ffn_in_pair_wait")
    ffn_in_to_send, ffn_in_own = pair_add("w_ffn_in", g_ffn_in, received)
    dattn = _matmul(dy_a, wts["w_o_attn"], mode="nt", tm=1024, tn=512, tk=256, out_dtype=F32, name="mm_dattn",
                    b_blocked=True)
    g["w_o_attn"] = _matmul(attn_b, dy_a, mode="tn", tm=512, tn=256, tk=2048, out_dtype=F32, name="mm_dwo",
                            out_blocked=N_CHIPS)
    du3 = _matmul(dy_b, wts["w_pw_conv"], mode="nt", tm=1024, tn=512, tk=256, out_dtype=F32, name="mm_du3",
                  b_blocked=True)
    g["w_pw_conv"] = _matmul(u3, dy_b, mode="tn", tm=512, tn=256, tk=2048, out_dtype=F32, name="mm_dwpw",
                             out_blocked=N_CHIPS)
    duc, g["conv_ln_w"], g["conv_ln_b"] = _ln_silu_bwd(du3, uc, wts["conv_ln_w"], wts["conv_ln_b"], "ln_bwd")

    small3 = ("w_out", "w_o_attn", "w_pw_conv")
    g_small3 = [_blocks_by_half(g.pop(n)) for n in small3]
    (da, db, g["conv_w"], g["conv_b"]), received = _conv_bwd(
        duc, u0, proj, wts["conv_w"], conv_col0, "conv_bwd", rider=_pair_exchange_rider(g_small3, halved=True))
    sums3 = [pair_add(n, gb, rv) for n, gb, rv in zip(small3, g_small3, received)]
    (dqn, dkn, dv), (by_chip["w_ffn_in"],) = _attn_bwd(
        qn, kn, proj, dattn, attn, lse, bd, "attn_bwd",
        rider=_chip_exchange_rider([ffn_in_to_send], [ffn_in_own]))
    (dproj, dqw, dkw), exchanged3 = _qk_bwd(
        dqn, dkn, dv, da, db, dgl, proj, pos_col, qw2, kw2, consts, "qk_bwd",
        rider=_chip_exchange_rider([s[0] for s in sums3], [s[1] for s in sums3]))
    by_chip.update(zip(small3, exchanged3))
    halves = [_sum_chips(by_chip[n], pos_arr, f"grads_chip_sum_{n}") for n in EARLY_REDUCE]
    g["q_norm_w"] = dqw[:, :HEAD_DIM]
    g["k_norm_w"] = dkw[:, :HEAD_DIM]

    c = pos_arr[0]
    rh = h.shape[1] // 2
    h_sibling = lax.dynamic_slice_in_dim(h, (1 - c) * rh, rh, axis=1)
    h_own = lax.dynamic_slice_in_dim(h, c * rh, rh, axis=1)
    g_sibling, shards = _matmul(h_sibling, dproj, mode="tn", tm=rh, tn=1920, tk=2048, out_dtype=F32,
                                name="mm_dwin_sibling", out_blocked=N_CHIPS, rider=_pair_gather_rider(halves))
    reduced = dict(zip(EARLY_REDUCE, shards))
    exchanging, started = _split_start(_pair_exchange_rider([g_sibling], halved=False), "grads_w_in_pair_start")
    g_own = _matmul(h_own, dproj, mode="tn", tm=rh, tn=1920, tk=2048, out_dtype=F32, name="mm_dwin_own",
                    out_blocked=N_CHIPS, after=[started])
    (from_sibling,), _ = _split_wait(exchanging, after=[g_own], name="grads_w_in_pair_wait")
    to_send, own = _add_own_half(g_own, from_sibling, pos_arr, "grads_pair_add_w_in")
    in_flight, started = _split_start(_chip_exchange_rider([to_send], [own]), "grads_w_in_exchange_start")
    dh = _matmul(dproj, wts["w_in"], mode="nt", tm=1024, tn=1024, tk=1920, out_dtype=F32, name="mm_dh",
                 b_blocked=True, after=[started])
    grad_x, _, g["norm1_w"] = _rmsnorm_bwd(dh, x, wts["norm1_w"], dx1, "rms1_bwd")
    return loss, grad_x, g, reduced, (in_flight, started)


def _mesh_pos():
    return lax.axis_index("x"), lax.axis_index("y"), lax.axis_index("c")


def _other_chips(x, y):
    return [(1 - x, y), (x, 1 - y), (1 - x, 1 - y)]


NEAR_CHIPS, FAR_CHIPS = (0, 1), (2,)


def _cast_into_slot(shard, chip_arr, dtype, name, n_slots=N_CHIPS, after=()):
    r, c = shard.shape
    tr = r // 2 if r % 32 == 0 else r

    def body(chip_ref, s_ref, *refs):
        refs[-1][...] = s_ref[...].astype(dtype)

    return pl.pallas_call(
        body, name=name,
        grid_spec=pltpu.PrefetchScalarGridSpec(
            num_scalar_prefetch=1, grid=(r // tr,),
            in_specs=[pl.BlockSpec((tr, c), lambda i, chip_ref: (i, 0))] + [ANY] * len(after),
            out_specs=pl.BlockSpec((None, tr, c), lambda i, chip_ref: (chip_ref[0], i, 0))),
        out_shape=jax.ShapeDtypeStruct((n_slots, r, c), dtype), compiler_params=_params(),
    )(chip_arr, shard, *after)


GATHER_CHUNKS = 4


def _gather_both_legs_rider(big, small, peers):
    nb = len(big)
    n = nb + len(small)
    nch = GATHER_CHUNKS

    def part(bufs, a, slot, half, ch):
        if a >= nb:
            return bufs[a].at[slot]
        rows = bufs[a].shape[2] // nch
        return bufs[a].at[slot, half, pl.ds(ch * rows, rows)]

    def pieces():
        return [(a, ch, k) for ch in range(nch) for a in range(n) for k in peers if a < nb or ch == 0]

    def ici(bufs, sems, a, ch, k, slot_of_src):
        x, y, c = _mesh_pos()
        px, py = _other_chips(x, y)[k]
        slot = 2 * x + y if slot_of_src == "mine" else 2 * px + py
        return pltpu.make_async_remote_copy(
            src_ref=part(bufs, a, slot, c, ch), dst_ref=part(bufs, a, slot, c, ch), send_sem=sems[0].at[a, ch, k],
            recv_sem=sems[1].at[a, ch, k], device_id=(px, py, c), device_id_type=MESH)

    def forward(bufs, sems, a, ch, k, half):
        x, y, c = _mesh_pos()
        px, py = _other_chips(x, y)[k]
        h = c if half == "mine" else 1 - c
        return pltpu.make_async_remote_copy(
            src_ref=part(bufs, a, 2 * px + py, h, ch), dst_ref=part(bufs, a, 2 * px + py, h, ch),
            send_sem=sems[2].at[a, ch, k], recv_sem=sems[3].at[a, ch, k], device_id=(x, y, 1 - c),
            device_id_type=MESH)

    def start(r_in, bufs, sems):
        for a, ch, k in pieces():
            ici(bufs, sems, a, ch, k, "mine").start()

    def middle(r_in, bufs, sems):
        for a, ch, k in pieces():
            ici(bufs, sems, a, ch, k, "theirs").wait_recv()
            if a < nb:
                forward(bufs, sems, a, ch, k, "mine").start()
        for a, ch, k in pieces():
            ici(bufs, sems, a, ch, k, "mine").wait_send()

    def wait(r_in, bufs, sems):
        for a, ch, k in pieces():
            if a < nb:
                forward(bufs, sems, a, ch, k, "theirs").wait_recv()
        for a, ch, k in pieces():
            if a < nb:
                forward(bufs, sems, a, ch, k, "mine").wait_send()

    ops = list(big) + list(small)
    return _Rider(ops, [jax.ShapeDtypeStruct(o.shape, o.dtype) for o in ops], {i: i for i in range(n)},
                  [pltpu.SemaphoreType.DMA((n, nch, 3)), pltpu.SemaphoreType.DMA((n, nch, 3)),
                   pltpu.SemaphoreType.DMA((nb, nch, 3)), pltpu.SemaphoreType.DMA((nb, nch, 3))],
                  start, wait, middle)


def _comm_call(rider, name):
    def body():
        pass

    return _pallas(body, name=name, grid=(1,), in_specs=[], out_specs=[], out_shape=[], operands=[],
                   rider=rider)[1]


def _gather_ici_rider(big, small):
    nb = len(big)
    n = nb + len(small)

    def copies(bufs, sems):
        x, y, c = _mesh_pos()
        me = 2 * x + y
        part = lambda a, slot: bufs[a].at[slot, c] if a < nb else bufs[a].at[slot]
        out = []
        for a in range(n):
            for k, (px, py) in enumerate(_other_chips(x, y)):
                send = functools.partial(
                    pltpu.make_async_remote_copy,
                    src_ref=part(a, me), dst_ref=part(a, me), send_sem=sems[0].at[a, k],
                    recv_sem=sems[1].at[a, k], device_id=(px, py, c), device_id_type=MESH)
                recv = functools.partial(
                    pltpu.make_async_remote_copy,
                    src_ref=part(a, 2 * px + py), dst_ref=part(a, 2 * px + py), send_sem=sems[0].at[a, k],
                    recv_sem=sems[1].at[a, k], device_id=(px, py, c), device_id_type=MESH)
                out.append((send, recv))
        return out

    def start(r_in, r_out, sems):
        for send, _ in copies(r_out, sems):
            send().start()

    def wait(r_in, r_out, sems):
        cps = copies(r_out, sems)
        for _, recv in cps:
            recv().wait_recv()
        for send, _ in cps:
            send().wait_send()

    ops = list(big) + list(small)
    return _Rider(ops, [jax.ShapeDtypeStruct(o.shape, o.dtype) for o in ops], {i: i for i in range(n)},
                  [pltpu.SemaphoreType.DMA((n, 3)), pltpu.SemaphoreType.DMA((n, 3))], start, wait)


def _gather_forward_rider(big):
    n = len(big)

    def copies(bufs, sems):
        x, y, c = _mesh_pos()
        out = []
        for a in range(n):
            for k, (px, py) in enumerate(_other_chips(x, y)):
                slot = 2 * px + py
                send = functools.partial(
                    pltpu.make_async_remote_copy,
                    src_ref=bufs[a].at[slot, c], dst_ref=bufs[a].at[slot, c], send_sem=sems[0].at[a, k],
                    recv_sem=sems[1].at[a, k], device_id=(x, y, 1 - c), device_id_type=MESH)
                recv = functools.partial(
                    pltpu.make_async_remote_copy,
                    src_ref=bufs[a].at[slot, 1 - c], dst_ref=bufs[a].at[slot, 1 - c], send_sem=sems[0].at[a, k],
                    recv_sem=sems[1].at[a, k], device_id=(x, y, 1 - c), device_id_type=MESH)
                out.append((send, recv))
        return out

    def start(r_in, r_out, sems):
        for send, _ in copies(r_out, sems):
            send().start()

    def wait(r_in, r_out, sems):
        cps = copies(r_out, sems)
        for _, recv in cps:
            recv().wait_recv()
        for send, _ in cps:
            send().wait_send()

    return _Rider(big, [jax.ShapeDtypeStruct(o.shape, o.dtype) for o in big], {i: i for i in range(n)},
                  [pltpu.SemaphoreType.DMA((n, 3)), pltpu.SemaphoreType.DMA((n, 3))], start, wait)


def _pair_exchange_rider(gs, halved):
    n = len(gs)

    def copies(r_in, r_out, sems):
        x, y, c = _mesh_pos()
        return [pltpu.make_async_remote_copy(
            src_ref=r_in[a].at[:, 1 - c] if halved else r_in[a], dst_ref=r_out[a], send_sem=sems[0].at[a],
            recv_sem=sems[1].at[a], device_id=(x, y, 1 - c), device_id_type=MESH) for a in range(n)]

    def start(r_in, r_out, sems):
        for cp in copies(r_in, r_out, sems):
            cp.start()

    def wait(r_in, r_out, sems):
        for cp in copies(r_in, r_out, sems):
            cp.wait()

    return _Rider(gs, [jax.ShapeDtypeStruct((g.shape[0],) + g.shape[-2:], g.dtype) for g in gs], {},
                  [pltpu.SemaphoreType.DMA((n,)), pltpu.SemaphoreType.DMA((n,))], start, wait)


def _chip_exchange_rider(to_send, by_chip, row_range=None):
    n = len(to_send)

    def copies(r_in, r_out, sems):
        x, y, c = _mesh_pos()
        me = 2 * x + y
        rows = (lambda ref: ref) if row_range is None else (lambda ref: ref.at[pl.ds(*row_range)])
        out = []
        for a in range(n):
            for k, (px, py) in enumerate(_other_chips(x, y)):
                send = functools.partial(
                    pltpu.make_async_remote_copy,
                    src_ref=rows(r_in[a].at[2 * px + py]), dst_ref=rows(r_out[a].at[me]),
                    send_sem=sems[0].at[a, k], recv_sem=sems[1].at[a, k], device_id=(px, py, c),
                    device_id_type=MESH)
                recv = functools.partial(
                    pltpu.make_async_remote_copy,
                    src_ref=rows(r_in[a].at[me]), dst_ref=rows(r_out[a].at[2 * px + py]),
                    send_sem=sems[0].at[a, k], recv_sem=sems[1].at[a, k], device_id=(px, py, c),
                    device_id_type=MESH)
                out.append((send, recv))
        return out

    def start(r_in, r_out, sems):
        for send, _ in copies(r_in, r_out, sems):
            send().start()

    def wait(r_in, r_out, sems):
        cps = copies(r_in, r_out, sems)
        for _, recv in cps:
            recv().wait_recv()
        for send, _ in cps:
            send().wait_send()

    return _Rider(list(to_send) + list(by_chip), [jax.ShapeDtypeStruct(b.shape, b.dtype) for b in by_chip],
                  {n + i: i for i in range(n)},
                  [pltpu.SemaphoreType.DMA((n, 3)), pltpu.SemaphoreType.DMA((n, 3))], start, wait)


HBM = pl.BlockSpec(memory_space=pltpu.HBM)
SEM = pl.BlockSpec(memory_space=pltpu.SEMAPHORE)


_IN_FLIGHT = pltpu.CompilerParams(has_side_effects=pltpu.SideEffectType.DATAFLOW_SIDE_EFFECTING)


class _FlatSems:
    def __init__(self, ref, shape):
        self.ref, self.shape = ref, shape

    @property
    def at(self):
        return self

    def __getitem__(self, idx):
        idx = idx if isinstance(idx, tuple) else (idx,)
        flat = 0
        for i, n in zip(idx, self.shape):
            flat = flat * n + i
        return self.ref.at[flat]


def _flat_sem_types(rider):
    return tuple(pltpu.SemaphoreType.DMA((int(np.prod(s.shape)),)) for s in rider.scratch)


def _as_rider_sems(rider, refs):
    return [_FlatSems(r, s.shape) for r, s in zip(refs, rider.scratch)]


def _split_start(rider, name, after=()):
    n_in, n_out, n_sem = len(rider.operands), len(rider.out_shapes), len(rider.scratch)
    n_after = len(after)
    fresh = [j for j in range(n_out) if j not in rider.aliases.values()]
    by_out = {j: i for i, j in rider.aliases.items()}

    def body(*refs):
        r_in = refs[:n_in]
        refs = refs[n_in + n_after:]
        sems = refs[:n_sem]
        thru = refs[n_sem:n_sem + n_in]
        fresh_refs = refs[n_sem + n_in:n_sem + n_in + len(fresh)]
        token = refs[-1]
        r_out = [thru[by_out[j]] if j in by_out else fresh_refs[fresh.index(j)] for j in range(n_out)]
        rider.start(r_in, r_out, _as_rider_sems(rider, sems))
        token[...] = jnp.zeros_like(token)

    res = pl.pallas_call(
        body, name=name,
        out_shape=_flat_sem_types(rider) + tuple(pltpu.HBM(o.shape, o.dtype) for o in rider.operands)
        + tuple(pltpu.HBM(rider.out_shapes[j].shape, rider.out_shapes[j].dtype) for j in fresh)
        + (jax.ShapeDtypeStruct((8, LANES), F32),),
        in_specs=(HBM,) * n_in + (ANY,) * n_after,
        out_specs=(SEM,) * n_sem + (HBM,) * (n_in + len(fresh)) + (pl.BlockSpec(memory_space=pltpu.VMEM),),
        input_output_aliases={i: n_sem + i for i in range(n_in)}, compiler_params=_IN_FLIGHT,
    )(*[pltpu.with_memory_space_constraint(o, pltpu.HBM) for o in rider.operands], *after)
    return (rider, res[:n_sem], res[n_sem:n_sem + n_in], res[n_sem + n_in:-1]), res[-1]


def _split_continue(handles, after, name, phase):
    rider, sems, thru, fresh_arrays = handles
    n_in, n_out, n_sem = len(rider.operands), len(rider.out_shapes), len(rider.scratch)
    fresh = [j for j in range(n_out) if j not in rider.aliases.values()]
    by_out = {j: i for i, j in rider.aliases.items()}
    n_data = n_in + len(fresh)

    def body(*refs):
        r_in = refs[:n_in]
        fresh_refs = refs[n_in:n_data]
        sem_refs = refs[n_data:n_data + n_sem]
        r_out = [r_in[by_out[j]] if j in by_out else fresh_refs[fresh.index(j)] for j in range(n_out)]
        phase(r_in, r_out, _as_rider_sems(rider, sem_refs))

    data = list(thru) + list(fresh_arrays)
    return pl.pallas_call(
        body, name=name, out_shape=tuple(pltpu.HBM(d.shape, d.dtype) for d in data),
        in_specs=(HBM,) * n_data + (SEM,) * n_sem + (ANY,) * len(after), out_specs=(HBM,) * n_data,
        input_output_aliases={i: i for i in range(n_data)}, compiler_params=_IN_FLIGHT,
    )(*data, *sems, *after)


def _split_middle(handles, after, name):
    rider, sems, thru, _ = handles
    res = _split_continue(handles, after, name, rider.middle)
    return rider, sems, res[:len(thru)], res[len(thru):]


def _split_wait(handles, after, name):
    rider = handles[0]
    n_in, n_out = len(rider.operands), len(rider.out_shapes)
    fresh = [j for j in range(n_out) if j not in rider.aliases.values()]
    by_out = {j: i for i, j in rider.aliases.items()}
    res = _split_continue(handles, after, name, rider.wait)
    return [res[by_out[j]] if j in by_out else res[n_in + fresh.index(j)] for j in range(n_out)], res[:n_in]


def _pair_gather_rider(bufs):
    n = len(bufs)

    def copies(r_out, sems):
        x, y, c = _mesh_pos()
        out = []
        for a in range(n):
            send = functools.partial(
                    pltpu.make_async_remote_copy,
                src_ref=r_out[a].at[c], dst_ref=r_out[a].at[c], send_sem=sems[0].at[a],
                recv_sem=sems[1].at[a], device_id=(x, y, 1 - c), device_id_type=MESH)
            recv = functools.partial(
                    pltpu.make_async_remote_copy,
                src_ref=r_out[a].at[1 - c], dst_ref=r_out[a].at[1 - c], send_sem=sems[0].at[a],
                recv_sem=sems[1].at[a], device_id=(x, y, 1 - c), device_id_type=MESH)
            out.append((send, recv))
        return out

    def start(r_in, r_out, sems):
        for send, _ in copies(r_out, sems):
            send().start()

    def wait(r_in, r_out, sems):
        cps = copies(r_out, sems)
        for _, recv in cps:
            recv().wait_recv()
        for send, _ in cps:
            send().wait_send()

    return _Rider(bufs, [jax.ShapeDtypeStruct(b.shape, b.dtype) for b in bufs], {i: i for i in range(n)},
                  [pltpu.SemaphoreType.DMA((n,)), pltpu.SemaphoreType.DMA((n,))], start, wait)


def _add_own_half(g, recv, pos_arr, name):
    nb, rh, cols = g.shape[0], g.shape[-2], g.shape[-1]

    def body(pos_ref, g_ref, r_ref, send_ref, own_ref):
        s = (g_ref[...] + r_ref[...]).astype(BF16)
        send_ref[...] = s

        @pl.when(pl.program_id(0) == pos_ref[1])
        def _():
            own_ref[...] = s

    blk = pl.BlockSpec((None, rh, cols), lambda j, pos_ref: (j, 0, 0))
    g_spec = blk if g.ndim == 3 else pl.BlockSpec((None, None, rh, cols),
                                                   lambda j, pos_ref: (j, pos_ref[0], 0, 0))
    shape = jax.ShapeDtypeStruct((nb, rh, cols), BF16)
    return pl.pallas_call(
        body, name=name,
        grid_spec=pltpu.PrefetchScalarGridSpec(
            num_scalar_prefetch=1, grid=(nb,), in_specs=[g_spec, blk],
            out_specs=[blk, pl.BlockSpec((None, rh, cols), lambda j, pos_ref: (pos_ref[1], 0, 0))]),
        out_shape=[shape, shape], compiler_params=_params(),
    )(pos_arr, g, recv)


def _sum_chips(gath, pos_arr, name):
    nb, rh, cols = gath.shape

    def body(pos_ref, a_ref, b_ref, c_ref, d_ref, o_ref):
        del pos_ref
        o_ref[...] = ((a_ref[...].astype(F32) + b_ref[...].astype(F32)) + c_ref[...].astype(F32)) \
            + d_ref[...].astype(F32)

    tr = rh // 2 if (rh // 2) % 16 == 0 else rh
    specs = [pl.BlockSpec((None, tr, cols), functools.partial(lambda i, pos_ref, j: (j, i, 0), j=j))
             for j in range(nb)]
    return pl.pallas_call(
        body, name=name,
        grid_spec=pltpu.PrefetchScalarGridSpec(
            num_scalar_prefetch=1, grid=(rh // tr,), in_specs=specs,
            out_specs=pl.BlockSpec((None, tr, cols), lambda i, pos_ref: (pos_ref[0], i, 0))),
        out_shape=jax.ShapeDtypeStruct((2, rh, cols), F32), compiler_params=_params(),
    )(pos_arr, gath, gath, gath, gath)


N_DEVICES = 8


def _small_gather_rider(buf):
    def copies(r_out, sems):
        x, y, c = _mesh_pos()
        me = 4 * x + 2 * y + c
        out = []
        for r in range(1, N_DEVICES):
            px = 1 - x if r & 4 else x
            py = 1 - y if r & 2 else y
            pc = 1 - c if r & 1 else c
            out.append(pltpu.make_async_remote_copy(
                src_ref=r_out[0].at[me], dst_ref=r_out[0].at[me], send_sem=sems[0].at[r - 1],
                recv_sem=sems[1].at[r - 1], device_id=(px, py, pc), device_id_type=MESH))
        return out

    def start(r_in, r_out, sems):
        for cp in copies(r_out, sems):
            cp.start()

    def wait(r_in, r_out, sems):
        cps = copies(r_out, sems)
        for cp in cps:
            cp.wait_recv()
        for cp in cps:
            cp.wait_send()

    return _Rider([buf], [jax.ShapeDtypeStruct(buf.shape, buf.dtype)], {0: 0},
                  [pltpu.SemaphoreType.DMA((N_DEVICES - 1,)), pltpu.SemaphoreType.DMA((N_DEVICES - 1,))],
                  start, wait)


def _sum_devices(buf, name):
    def body(b_ref, o_ref):
        acc = b_ref[0]
        for i in range(1, N_DEVICES):
            acc = acc + b_ref[i]
        o_ref[...] = acc

    return _pallas(body, name=name, grid=(1,), in_specs=[pl.BlockSpec(buf.shape, lambda i: (0, 0, 0))],
                   out_specs=pl.BlockSpec(buf.shape[1:], lambda i: (0, 0)),
                   out_shape=jax.ShapeDtypeStruct(buf.shape[1:], F32), operands=[buf])


def _adamw_math(w, g, m, v):
    m = ADAM_B1 * m + (1.0 - ADAM_B1) * g
    v = ADAM_B2 * v + (1.0 - ADAM_B2) * (g * g)
    m_hat = m / (1.0 - ADAM_B1 ** ADAM_STEP)
    v_hat = v / (1.0 - ADAM_B2 ** ADAM_STEP)
    delta = -ADAM_LR * (m_hat / (jnp.sqrt(v_hat) + ADAM_EPS) + ADAM_WD * w)
    return delta, m, v


def _adamw(w, g, m, v, name, after=()):
    r, c = w.shape
    tr = 128 if r % 128 == 0 else 64
    assert r % tr == 0

    def body(w_ref, g_ref, m_ref, v_ref, go_ref, d_ref, mo_ref, vo_ref):
        gv = g_ref[...]
        d, mn, vn = _adamw_math(w_ref[...], gv, m_ref[...], v_ref[...])
        go_ref[...] = gv
        d_ref[...] = d
        mo_ref[...] = mn
        vo_ref[...] = vn

    blk = pl.BlockSpec((tr, c), lambda i: (i, 0))
    return _pallas(body, name=name, grid=(r // tr,), in_specs=[blk] * 4, out_specs=[blk] * 4,
                   out_shape=[jax.ShapeDtypeStruct((r, c), F32)] * 4, operands=[w, g, m, v], after=after)


def _adamw_small(ws, gs, ms, vs, name):
    n = len(ws)

    def body(*refs):
        w_r, g_r, m_r, v_r = refs[:n], refs[n:2 * n], refs[2 * n:3 * n], refs[3 * n:4 * n]
        d_o, m_o, v_o = refs[4 * n:5 * n], refs[5 * n:6 * n], refs[6 * n:7 * n]
        for i in range(n):
            d, mn, vn = _adamw_math(w_r[i][...], g_r[i][...], m_r[i][...], v_r[i][...])
            d_o[i][...] = d
            m_o[i][...] = mn
            v_o[i][...] = vn

    specs = [pl.BlockSpec(w.shape, lambda i: (0, 0)) for w in ws]
    shapes = [jax.ShapeDtypeStruct(w.shape, F32) for w in ws]
    outs = pl.pallas_call(
        body, name=name, grid=(1,), in_specs=specs * 4, out_specs=specs * 3, out_shape=shapes * 3,
        compiler_params=_params(),
    )(*ws, *gs, *ms, *vs)
    return outs[:n], outs[n:2 * n], outs[2 * n:]


BIG = ("w_in", "w_o_attn", "w_pw_conv", "w_out", "w_ffn_in", "w_ffn_out")
ROW_SHARDED = ("w_out", "w_ffn_out")
SMALL = ("norm1_w", "b_gate", "q_norm_w", "k_norm_w", "conv_w", "conv_b", "conv_ln_w", "conv_ln_b", "norm2_w")
ORDER = ("norm1_w", "w_in", "b_gate", "q_norm_w", "k_norm_w", "w_o_attn", "conv_w", "conv_b", "conv_ln_w",
         "conv_ln_b", "w_pw_conv", "w_out", "norm2_w", "w_ffn_in", "w_ffn_out")
PACK_TILE = 8 * LANES


def _pack_small(parts):
    rows = []
    for p in parts:
        flat = p.reshape(-1)
        pad = (-flat.shape[0]) % PACK_TILE
        rows.append(jnp.pad(flat, (0, pad)).reshape(-1, LANES))
    return jnp.concatenate(rows, axis=0)


def _unpack_small(packed, shapes):
    out, row = [], 0
    for shp in shapes:
        size = int(np.prod(shp))
        nrow = -(-size // PACK_TILE) * (PACK_TILE // LANES)
        out.append(packed[row:row + nrow].reshape(-1)[:size].reshape(shp))
        row += nrow
    return out


def kernel(x, positions, norm1_w, w_in, b_gate, q_norm_w, k_norm_w, w_o_attn, conv_w, conv_b, conv_ln_w, conv_ln_b, w_pw_conv, w_out, norm2_w, w_ffn_in, w_ffn_out, loss_target, m_norm1_w, m_w_in, m_b_gate, m_q_norm_w, m_k_norm_w, m_w_o_attn, m_conv_w, m_conv_b, m_conv_ln_w, m_conv_ln_b, m_w_pw_conv, m_w_out, m_norm2_w, m_w_ffn_in, m_w_ffn_out, v_norm1_w, v_w_in, v_b_gate, v_q_norm_w, v_k_norm_w, v_w_o_attn, v_conv_w, v_conv_b, v_conv_ln_w, v_conv_ln_b, v_w_pw_conv, v_w_out, v_norm2_w, v_w_ffn_in, v_w_ffn_out):
    w = dict(norm1_w=norm1_w, w_in=w_in, b_gate=b_gate, q_norm_w=q_norm_w, k_norm_w=k_norm_w, w_o_attn=w_o_attn,
             conv_w=conv_w, conv_b=conv_b, conv_ln_w=conv_ln_w, conv_ln_b=conv_ln_b, w_pw_conv=w_pw_conv,
             w_out=w_out, norm2_w=norm2_w, w_ffn_in=w_ffn_in, w_ffn_out=w_ffn_out)
    m = dict(norm1_w=m_norm1_w, w_in=m_w_in, b_gate=m_b_gate, q_norm_w=m_q_norm_w, k_norm_w=m_k_norm_w,
             w_o_attn=m_w_o_attn, conv_w=m_conv_w, conv_b=m_conv_b, conv_ln_w=m_conv_ln_w,
             conv_ln_b=m_conv_ln_b, w_pw_conv=m_w_pw_conv, w_out=m_w_out, norm2_w=m_norm2_w,
             w_ffn_in=m_w_ffn_in, w_ffn_out=m_w_ffn_out)
    v = dict(norm1_w=v_norm1_w, w_in=v_w_in, b_gate=v_b_gate, q_norm_w=v_q_norm_w, k_norm_w=v_k_norm_w,
             w_o_attn=v_w_o_attn, conv_w=v_conv_w, conv_b=v_conv_b, conv_ln_w=v_conv_ln_w,
             conv_ln_b=v_conv_ln_b, w_pw_conv=v_w_pw_conv, w_out=v_w_out, norm2_w=v_norm2_w,
             w_ffn_in=v_w_ffn_in, w_ffn_out=v_w_ffn_out)
    cx, cy, cc = _mesh_pos()
    chip = 2 * cx + cy

    chip_arr = chip.reshape(1).astype(jnp.int32)
    pos_arr = jnp.stack([cc, chip]).astype(jnp.int32)
    halves = lambda buf: buf.reshape(N_CHIPS, 2, buf.shape[1] // 2, buf.shape[2])
    w_in_buf = halves(_cast_into_slot(w["w_in"][0], chip_arr, BF16, "cast_w_in"))
    small_bufs = [_cast_into_slot(w[n][0], chip_arr, F32, f"slot_{n}") for n in ("conv_w", "b_gate")]
    first_gather, started = _split_start(_gather_both_legs_rider([w_in_buf], small_bufs, NEAR_CHIPS),
                                         "allgather_w_in_near_start")
    late_bufs = [halves(_cast_into_slot(w[n][0], chip_arr, BF16, f"cast_{n}", after=[started]))
                 for n in LATE_GATHER]
    wts = dict(norm1_w=norm1_w, q_norm_w=q_norm_w, k_norm_w=k_norm_w, conv_b=conv_b, conv_ln_w=conv_ln_w,
               conv_ln_b=conv_ln_b, norm2_w=norm2_w)

    loss, grad_x, g, reduced, w_in_in_flight = _forward_backward(
        x[0], positions.reshape(-1, 1), loss_target[0], wts, first_gather, late_bufs, pos_arr)
    grads = {n: b.reshape(-1, b.shape[2]) for n, b in reduced.items()}

    w_in_in_flight, started = w_in_in_flight
    delta, new_m, new_v = {}, {}, {}
    for n in EARLY_REDUCE:
        grads[n], delta[n], new_m[n], new_v[n] = _adamw(w[n][0], grads[n], m[n][0], v[n][0], f"adamw_{n}",
                                                        after=[started])
    small_parts = [loss] + [g[n] for n in SMALL]
    small_shapes = [p.shape for p in small_parts]
    device_arr = (4 * cx + 2 * cy + cc).reshape(1).astype(jnp.int32)
    small_buf = _cast_into_slot(_pack_small(small_parts), device_arr, F32, "slot_small", n_slots=N_DEVICES)

    (by_chip_w_in,), _ = _split_wait(w_in_in_flight, after=[delta[n] for n in EARLY_REDUCE] + [small_buf],
                                     name="grads_w_in_exchange_wait")
    half_w_in = _sum_chips(by_chip_w_in, pos_arr, "grads_chip_sum_w_in")
    shard_w_in, small_buf = _comm_call(
        _riders_together(_pair_gather_rider([half_w_in]), _small_gather_rider(small_buf)),
        "grads_pair_gather_w_in_small_gather")
    summed = _sum_devices(small_buf, "small_sum")
    reduced = _unpack_small(summed, small_shapes)
    loss_total = reduced[0].reshape(())
    for n, r in zip(SMALL, reduced[1:]):
        grads[n] = r
    ch_shard = conv_w.shape[2]
    grads["conv_w"] = lax.dynamic_slice_in_dim(grads["conv_w"], chip * ch_shard, ch_shard, axis=1)
    d_shard = b_gate.shape[2]
    grads["b_gate"] = lax.dynamic_slice_in_dim(grads["b_gate"], chip * d_shard, d_shard, axis=1)

    grads["w_in"], delta["w_in"], new_m["w_in"], new_v["w_in"] = _adamw(
        w["w_in"][0], shard_w_in.reshape(-1, shard_w_in.shape[2]), m["w_in"][0], v["w_in"][0], "adamw_w_in")
    flat2 = lambda a: a.reshape(-1, a.shape[-1])
    d_s, m_s, v_s = _adamw_small([flat2(w[n]) for n in SMALL], [flat2(grads[n]) for n in SMALL],
                                 [flat2(m[n]) for n in SMALL], [flat2(v[n]) for n in SMALL], "adamw_small")
    for i, n in enumerate(SMALL):
        delta[n], new_m[n], new_v[n] = d_s[i], m_s[i], v_s[i]

    shaped = lambda d, n: d[n].reshape(w[n].shape)
    return (loss_total, grad_x[None], *[shaped(grads, n) for n in ORDER], *[shaped(delta, n) for n in ORDER],
            *[shaped(new_m, n) for n in ORDER], *[shaped(new_v, n) for n in ORDER])
```

```python
import functools

import numpy as np
import jax
import jax.numpy as jnp
from jax import lax
from jax.experimental import pallas as pl
from jax.experimental.pallas import tpu as pltpu

F32 = jnp.float32
BF16 = jnp.bfloat16
MESH = pl.DeviceIdType.MESH
ANY = pl.BlockSpec(memory_space=pl.ANY)

HEAD_DIM = 64
N_SLOT_HEADS = 8
DILATIONS = (1, 4, 16)
HALF_SPAN = 64
ROPE_THETA = 500000.0
ROT_DIM = 16
CONV_WIDTH = 31
EPS = 1e-6
NEG_INF = -1e30
ADAM_LR, ADAM_B1, ADAM_B2, ADAM_EPS, ADAM_WD, ADAM_STEP = 0.001, 0.9, 0.999, 1e-08, 0.01, 10

LANES = 128
QBLK = 128
KWIN = QBLK + 2 * HALF_SPAN
VMEM_LIMIT = 48 * 1024 * 1024
N_CHIPS = 4


def _params(**kw):
    return pltpu.CompilerParams(vmem_limit_bytes=VMEM_LIMIT, **kw)


class _Rider:
    def __init__(self, operands, out_shapes, aliases, scratch, start, wait, middle=None):
        self.operands, self.out_shapes, self.aliases = list(operands), list(out_shapes), dict(aliases)
        self.scratch, self.start, self.wait = list(scratch), start, wait
        self.middle = middle


def _riders_together(a, b):
    n_in, n_out, n_sc = len(a.operands), len(a.out_shapes), len(a.scratch)
    aliases = dict(a.aliases)
    aliases.update({n_in + src: n_out + dst for src, dst in b.aliases.items()})

    def start(r_in, r_out, r_sc):
        a.start(r_in[:n_in], r_out[:n_out], r_sc[:n_sc])
        b.start(r_in[n_in:], r_out[n_out:], r_sc[n_sc:])

    def wait(r_in, r_out, r_sc):
        a.wait(r_in[:n_in], r_out[:n_out], r_sc[:n_sc])
        b.wait(r_in[n_in:], r_out[n_out:], r_sc[n_sc:])

    return _Rider(a.operands + b.operands, a.out_shapes + b.out_shapes, aliases, a.scratch + b.scratch, start, wait)


def _pallas(body, *, name, grid, in_specs, out_specs, out_shape, operands, scratch_shapes=(), aliases=None,
            rider=None, after=()):
    single = not isinstance(out_specs, (list, tuple))
    out_specs_l = [out_specs] if single else list(out_specs)
    out_shape_l = [out_shape] if single else list(out_shape)
    aliases = dict(aliases or {})

    def call(fn, all_in_specs, all_out_specs, all_out_shape, all_scratch, all_aliases, all_operands):
        return pl.pallas_call(
            fn, name=name, grid=grid, in_specs=all_in_specs, out_specs=all_out_specs, out_shape=all_out_shape,
            scratch_shapes=all_scratch, input_output_aliases=all_aliases, compiler_params=_params(),
        )(*all_operands)

    if rider is None:
        n_main = len(in_specs)

        def ordered(*refs):
            body(*refs[:n_main], *refs[n_main + len(after):])

        res = call(ordered if after else body, list(in_specs) + [ANY] * len(after), out_specs_l, out_shape_l,
                   list(scratch_shapes), aliases, list(operands) + list(after))
        return res[0] if single else res
    assert not after
    n_in, n_rin = len(in_specs), len(rider.operands)
    n_out, n_rout = len(out_specs_l), len(rider.out_shapes)
    n_sc = len(scratch_shapes)

    def wrapped(*refs):
        main_in, r_in = refs[:n_in], refs[n_in:n_in + n_rin]
        o0 = n_in + n_rin
        main_out, r_out = refs[o0:o0 + n_out], refs[o0 + n_out:o0 + n_out + n_rout]
        s0 = o0 + n_out + n_rout
        main_sc, r_sc = refs[s0:s0 + n_sc], refs[s0 + n_sc:]
        ids = [pl.program_id(d) for d in range(len(grid))]
        first = functools.reduce(jnp.logical_and, [i == 0 for i in ids])
        last = functools.reduce(jnp.logical_and, [i == n - 1 for i, n in zip(ids, grid)])

        @pl.when(first)
        def _():
            rider.start(r_in, r_out, r_sc)

        body(*main_in, *main_out, *main_sc)

        @pl.when(last)
        def _():
            rider.wait(r_in, r_out, r_sc)

    for src, dst in rider.aliases.items():
        aliases[n_in + src] = n_out + dst
    res = call(wrapped, list(in_specs) + [ANY] * n_rin, out_specs_l + [ANY] * n_rout,
               out_shape_l + rider.out_shapes, list(scratch_shapes) + rider.scratch, aliases,
               list(operands) + rider.operands)
    main = res[:n_out]
    return (main[0] if single else main), res[n_out:]


def _matmul(a, b, *, mode, tm, tn, tk, out_dtype, name, b_blocked=False,
            out_blocked=None, cols_outer=False, rider=None, after=()):
    a_shape = a.shape
    if mode == "nn":
        m_dim, k_dim = a_shape
        n_dim = b.shape[0] * b.shape[2] if b_blocked else b.shape[1]
        rows, cols, red = m_dim, n_dim, k_dim
    elif mode == "nt":
        m_dim, n_dim = a_shape
        k_dim = b.shape[1] if b_blocked else b.shape[0]
        rows, cols, red = m_dim, k_dim, n_dim
    else:
        m_dim, k_dim = a_shape
        n_dim = b.shape[1]
        rows, cols, red = k_dim, n_dim, m_dim
    assert rows % tm == 0 and cols % tn == 0 and red % tk == 0, (name, rows, cols, red)
    ni, nj, nk = rows // tm, cols // tn, red // tk

    if mode == "nn":
        a_spec = pl.BlockSpec((tm, tk), lambda i, j, k: (i, k))
        if b_blocked:
            per = b.shape[2] // tn
            b_spec = pl.BlockSpec((None, tk, tn), lambda i, j, k: (j // per, k, j % per))
        else:
            b_spec = pl.BlockSpec((tk, tn), lambda i, j, k: (k, j))
        dims = (((1,), (0,)), ((), ()))
    elif mode == "nt":
        a_spec = pl.BlockSpec((tm, tk), lambda i, j, k: (i, k))
        if b_blocked:
            per = b.shape[2] // tk
            b_spec = pl.BlockSpec((None, tn, tk), lambda i, j, k: (k // per, j, k % per))
        else:
            b_spec = pl.BlockSpec((tn, tk), lambda i, j, k: (j, k))
        dims = (((1,), (1,)), ((), ()))
    else:
        a_spec = pl.BlockSpec((tk, tm), lambda i, j, k: (k, i))
        b_spec = pl.BlockSpec((tk, tn), lambda i, j, k: (k, j))
        dims = (((0,), (0,)), ((), ()))

    if out_blocked:
        per_o = (cols // out_blocked) // tn
        out_spec = pl.BlockSpec((None, tm, tn), lambda i, j, k: (j // per_o, i, j % per_o))
        out_shape = jax.ShapeDtypeStruct((out_blocked, rows, cols // out_blocked), out_dtype)
    else:
        out_spec = pl.BlockSpec((tm, tn), lambda i, j, k: (i, j))
        out_shape = jax.ShapeDtypeStruct((rows, cols), out_dtype)

    def body(a_ref, b_ref, o_ref, *acc):
        prod = lax.dot_general(a_ref[...], b_ref[...], dims, preferred_element_type=F32)
        if nk == 1:
            o_ref[...] = prod.astype(out_dtype)
        else:
            acc_ref, = acc
            k = pl.program_id(2)

            @pl.when(k == 0)
            def _():
                acc_ref[...] = prod

            @pl.when(k > 0)
            def _():
                acc_ref[...] += prod

            @pl.when(k == nk - 1)
            def _():
                o_ref[...] = acc_ref[...].astype(out_dtype)

    scratch = [pltpu.VMEM((tm, tn), F32)] if nk > 1 else []
    grid = (ni, nj, nk)
    if cols_outer:
        swap = lambda spec: pl.BlockSpec(spec.block_shape, lambda j, i, k, f=spec.index_map: f(i, j, k))
        a_spec, b_spec, out_spec, grid = swap(a_spec), swap(b_spec), swap(out_spec), (nj, ni, nk)
    return _pallas(body, name=name, grid=grid, in_specs=[a_spec, b_spec], out_specs=out_spec,
                   out_shape=out_shape, operands=[a, b], scratch_shapes=scratch, rider=rider, after=after)


def _proj_by_slot(h, w_in, order, first, count, name, proj=None, after=()):
    s, k = h.shape
    nb = w_in.shape[2]
    tm = 512
    prev = [] if proj is None else [proj]

    def body(order_ref, h_ref, w_ref, *refs):
        refs[-1][...] = jnp.dot(h_ref[...], w_ref[...], preferred_element_type=F32)

    return pl.pallas_call(
        body, name=name,
        grid_spec=pltpu.PrefetchScalarGridSpec(
            num_scalar_prefetch=1, grid=(count, s // tm),
            in_specs=[pl.BlockSpec((tm, k), lambda j, i, order_ref: (i, 0)),
                      pl.BlockSpec((None, k, nb), lambda j, i, order_ref: (order_ref[first + j], 0, 0))]
            + [ANY] * (len(prev) + len(after)),
            out_specs=pl.BlockSpec((tm, nb), lambda j, i, order_ref: (i, order_ref[first + j]))),
        out_shape=jax.ShapeDtypeStruct((s, N_CHIPS * nb), F32),
        input_output_aliases={3: 0} if prev else {}, compiler_params=_params(),
    )(order, h, w_in, *prev, *after)


def _rmsnorm_fwd(x, w, name):
    s, d = x.shape
    tm = 256

    def body(x_ref, w_ref, o_ref):
        xv = x_ref[...]
        rstd = lax.rsqrt(jnp.mean(xv * xv, axis=-1, keepdims=True) + EPS)
        o_ref[...] = (xv * rstd * w_ref[...]).astype(BF16)

    return pl.pallas_call(
        body, name=name, grid=(s // tm,),
        in_specs=[pl.BlockSpec((tm, d), lambda i: (i, 0)), pl.BlockSpec((1, d), lambda i: (0, 0))],
        out_specs=pl.BlockSpec((tm, d), lambda i: (i, 0)),
        out_shape=jax.ShapeDtypeStruct((s, d), BF16), compiler_params=_params(),
    )(x, w)


def _rmsnorm_bwd(dh, x, w, dres, name, rider=None):
    s, d = x.shape
    tm = 256

    def body(dh_ref, x_ref, w_ref, dres_ref, dx_ref, dxb_ref, dw_ref):
        xv = x_ref[...]
        rstd = lax.rsqrt(jnp.mean(xv * xv, axis=-1, keepdims=True) + EPS)
        xhat = xv * rstd
        dhv = dh_ref[...]
        g = dhv * w_ref[...]
        dx = rstd * (g - xhat * jnp.mean(g * xhat, axis=-1, keepdims=True)) + dres_ref[...]
        dx_ref[...] = dx
        dxb_ref[...] = dx.astype(BF16)
        part = jnp.sum(dhv * xhat, axis=0, keepdims=True)

        @pl.when(pl.program_id(0) == 0)
        def _():
            dw_ref[...] = part

        @pl.when(pl.program_id(0) > 0)
        def _():
            dw_ref[...] += part

    row = pl.BlockSpec((tm, d), lambda i: (i, 0))
    vec = pl.BlockSpec((1, d), lambda i: (0, 0))
    return _pallas(
        body, name=name, grid=(s // tm,), in_specs=[row, row, vec, row], out_specs=[row, row, vec],
        out_shape=[jax.ShapeDtypeStruct((s, d), F32), jax.ShapeDtypeStruct((s, d), BF16),
                   jax.ShapeDtypeStruct((1, d), F32)],
        operands=[dh, x, w, dres], rider=rider)


def _rope_consts():
    lane = np.arange(LANES)
    in_head = lane % HEAD_DIM
    inv_freq = ROPE_THETA ** (-jnp.arange(0, ROT_DIM, 2, dtype=F32) / ROT_DIM)
    invf = jnp.where(jnp.asarray(in_head < ROT_DIM), jnp.tile(inv_freq, LANES // (ROT_DIM // 2)), 0.0)
    m_a = np.where(in_head < ROT_DIM // 2, -1.0, 0.0).astype(np.float32)
    m_b = np.where((in_head >= ROT_DIM // 2) & (in_head < ROT_DIM), 1.0, 0.0).astype(np.float32)
    block_diag = (lane[:, None] // HEAD_DIM == lane[None, :] // HEAD_DIM).astype(np.float32)
    return (invf.reshape(1, LANES).astype(F32), jnp.asarray(m_a).reshape(1, LANES),
            jnp.asarray(m_b).reshape(1, LANES), jnp.asarray(block_diag, dtype=BF16))


def _head_sums(v, bd):
    hi = v.astype(BF16)
    lo = (v - hi.astype(F32)).astype(BF16)
    return jnp.dot(hi, bd, preferred_element_type=F32) + jnp.dot(lo, bd, preferred_element_type=F32)


def _qk_fwd(proj, pos_col, qw2, kw2, consts, name, rider=None):
    s = proj.shape[0]
    width = 3 * N_SLOT_HEADS * HEAD_DIM
    tm = 128
    invf, m_a, m_b, bd = consts
    scale = HEAD_DIM ** -0.5

    def body(q_ref, k_ref, pos_ref, qw_ref, kw_ref, invf_ref, ma_ref, mb_ref, bd_ref, qo_ref, ko_ref):
        ang = pos_ref[...].astype(F32) * invf_ref[...]
        cos = jnp.cos(ang)
        sin = jnp.sin(ang)
        s_a = sin * ma_ref[...]
        s_b = sin * mb_ref[...]
        bdv = bd_ref[...]
        for src, w_ref, dst, sc in ((q_ref, qw_ref, qo_ref, scale), (k_ref, kw_ref, ko_ref, 1.0)):
            for cb in range(width // LANES):
                cols = slice(cb * LANES, (cb + 1) * LANES)
                t = src[:, cols]
                rstd = lax.rsqrt(_head_sums(t * t, bdv) * (1.0 / HEAD_DIM) + EPS)
                y = t * rstd * w_ref[...]
                r = y * cos + pltpu.roll(y, LANES - 8, axis=1) * s_a + pltpu.roll(y, 8, axis=1) * s_b
                dst[:, cols] = r * sc if sc != 1.0 else r

    vec = pl.BlockSpec((1, LANES), lambda i: (0, 0))
    return _pallas(
        body, name=name, grid=(s // tm,),
        in_specs=[pl.BlockSpec((tm, width), lambda i: (i, 0)), pl.BlockSpec((tm, width), lambda i: (i, 1)),
                  pl.BlockSpec((tm, 1), lambda i: (i, 0)), vec, vec, vec, vec, vec,
                  pl.BlockSpec((LANES, LANES), lambda i: (0, 0))],
        out_specs=[pl.BlockSpec((tm, width), lambda i: (i, 0))] * 2,
        out_shape=[jax.ShapeDtypeStruct((s, width), F32)] * 2,
        operands=[proj, proj, pos_col, qw2, kw2, invf, m_a, m_b, bd], rider=rider)


def _qk_bwd(dqn, dkn, dv, da, db, dgl, proj, pos_col, qw2, kw2, consts, name, rider=None):
    s = proj.shape[0]
    width = 3 * N_SLOT_HEADS * HEAD_DIM
    ch = da.shape[1]
    gate_w = dgl.shape[1]
    out_w = 3 * width + 2 * ch + gate_w
    assert out_w == proj.shape[1]
    tm = 128
    invf, m_a, m_b, bd = consts
    scale = HEAD_DIM ** -0.5

    def body(dq_ref, dk_ref, dv_ref, da_ref, db_ref, dgl_ref, q_ref, k_ref, pos_ref, qw_ref, kw_ref,
             invf_ref, ma_ref, mb_ref, bd_ref, out_ref, dqw_ref, dkw_ref):
        ang = pos_ref[...].astype(F32) * invf_ref[...]
        cos = jnp.cos(ang)
        sin = jnp.sin(ang)
        s_a = sin * ma_ref[...]
        s_b = sin * mb_ref[...]
        bdv = bd_ref[...]
        first = pl.program_id(0) == 0
        for src, dsrc, w_ref, col0, dw_ref, sc in ((q_ref, dq_ref, qw_ref, 0, dqw_ref, scale),
                                                   (k_ref, dk_ref, kw_ref, width, dkw_ref, 1.0)):
            dw_acc = jnp.zeros((1, LANES), F32)
            for cb in range(width // LANES):
                cols = slice(cb * LANES, (cb + 1) * LANES)
                t = src[:, cols]
                dr = dsrc[:, cols]
                if sc != 1.0:
                    dr = dr * sc
                dy = dr * cos + pltpu.roll(dr * s_a, 8, axis=1) + pltpu.roll(dr * s_b, LANES - 8, axis=1)
                rstd = lax.rsqrt(_head_sums(t * t, bdv) * (1.0 / HEAD_DIM) + EPS)
                xhat = t * rstd
                g = dy * w_ref[...]
                dt = rstd * (g - xhat * (_head_sums(g * xhat, bdv) * (1.0 / HEAD_DIM)))
                out_ref[:, col0 + cb * LANES: col0 + (cb + 1) * LANES] = dt.astype(BF16)
                dw_acc = dw_acc + jnp.sum(dy * xhat, axis=0, keepdims=True)
            dw_acc = dw_acc + pltpu.roll(dw_acc, HEAD_DIM, axis=1)

            @pl.when(first)
            def _(dw_ref=dw_ref, dw_acc=dw_acc):
                dw_ref[...] = dw_acc

            @pl.when(jnp.logical_not(first))
            def _(dw_ref=dw_ref, dw_acc=dw_acc):
                dw_ref[...] += dw_acc
        out_ref[:, 2 * width: 3 * width] = dv_ref[...].astype(BF16)
        out_ref[:, 3 * width: 3 * width + ch] = da_ref[...]
        out_ref[:, 3 * width + ch: 3 * width + 2 * ch] = db_ref[...]
        out_ref[:, 3 * width + 2 * ch: out_w] = dgl_ref[...]

    vec = pl.BlockSpec((1, LANES), lambda i: (0, 0))
    blk = lambda c: pl.BlockSpec((tm, width), lambda i: (i, c))
    cblk = pl.BlockSpec((tm, ch), lambda i: (i, 0))
    return _pallas(
        body, name=name, grid=(s // tm,),
        in_specs=[blk(0), blk(0), blk(0), cblk, cblk, pl.BlockSpec((tm, gate_w), lambda i: (i, 0)),
                  blk(0), blk(1), pl.BlockSpec((tm, 1), lambda i: (i, 0)), vec, vec, vec, vec, vec,
                  pl.BlockSpec((LANES, LANES), lambda i: (0, 0))],
        out_specs=[pl.BlockSpec((tm, out_w), lambda i: (i, 0)), vec, vec],
        out_shape=[jax.ShapeDtypeStruct((s, out_w), BF16)] + [jax.ShapeDtypeStruct((1, LANES), F32)] * 2,
        operands=[dqn, dkn, dv, da, db, dgl, proj, proj, pos_col, qw2, kw2, invf, m_a, m_b, bd],
        rider=rider)


def _row_chunks(n_rows, fn, chunk=256):
    def step(i, c):
        fn(pl.ds(pl.multiple_of(i * chunk, chunk), chunk))
        return c
    lax.fori_loop(0, n_rows // chunk, step, 0)


def _to_residue_major(dst, src, s, d, dst_off=0, cast=None):
    seq = s // d
    for r in range(d):
        v = src[...] if d == 1 else src[pl.ds(r, seq, stride=d), :]
        dst[dst_off + r * seq: dst_off + (r + 1) * seq, :] = v if cast is None else v.astype(cast)


def _from_residue_major(dst, src, s, d, src_off=0):
    seq = s // d
    for r in range(d):
        v = src[src_off + r * seq: src_off + (r + 1) * seq, :]
        if d == 1:
            dst[...] = v
        else:
            dst[pl.ds(r, seq, stride=d), :] = v


def _band_bias():
    qi = lax.broadcasted_iota(jnp.int32, (QBLK, KWIN), 0)
    kj = lax.broadcasted_iota(jnp.int32, (QBLK, KWIN), 1)
    return jnp.where(jnp.abs(kj - HALF_SPAN - qi) <= HALF_SPAN, 0.0, NEG_INF).astype(F32)


def _range_bias(base, seq):
    kj = lax.broadcasted_iota(jnp.int32, (1, KWIN), 1)
    lo = (base & -seq) - base + HALF_SPAN
    return jnp.where((kj >= lo) & (kj < lo + seq), 0.0, NEG_INF).astype(F32)


def _skewed_blocks(n_blk, produce, consume):
    produce(0, 0)
    for b in range(n_blk):
        consume(b, b % 2)
        if b + 1 < n_blk:
            produce(b + 1, (b + 1) % 2)


def _block_base(b):
    return b * QBLK if isinstance(b, int) else pl.multiple_of(b * QBLK, QBLK)


def _attn_fwd(qn, kn, proj, name, rider=None):
    s = qn.shape[0]
    n_pairs = N_SLOT_HEADS * HEAD_DIM // LANES
    v_col0 = 2 * qn.shape[1] // LANES
    nt_dims = (((1,), (1,)), ((), ()))

    def body(q_ref, k_ref, v_ref, attn_ref, lse_ref, attn_b_ref, q_rm, k_rm, v_rm, acc_rm, m_rm, l_rm,
             acc_p, m_p, l_p, m_run, l_run, acc_run, band, s_buf, m_buf):
        g = pl.program_id(1)
        zpad = jnp.zeros((HALF_SPAN, LANES), BF16)
        k_rm[0:HALF_SPAN, :] = zpad
        k_rm[s + HALF_SPAN: s + 2 * HALF_SPAN, :] = zpad
        v_rm[0:HALF_SPAN, 0:LANES] = zpad
        v_rm[s + HALF_SPAN: s + 2 * HALF_SPAN, 0:LANES] = zpad

        def ones_rows(rows):
            v_rm[pl.ds(rows.start, rows.size), LANES:2 * LANES] = jnp.ones((rows.size, LANES), BF16)

        _row_chunks(s + 2 * HALF_SPAN, ones_rows, chunk=2 * HALF_SPAN)
        band[...] = _band_bias()
        lane = lax.broadcasted_iota(jnp.int32, (QBLK, LANES), 1)
        low = lane < HEAD_DIM
        n_blk = s // QBLK

        for gi, d in enumerate(DILATIONS):
            @pl.when(g == gi)
            def _(gi=gi, d=d):
                seq = s // d
                _to_residue_major(q_rm, q_ref, s, d, cast=BF16)
                _to_residue_major(k_rm, k_ref, s, d, dst_off=HALF_SPAN, cast=BF16)
                _to_residue_major(v_rm.at[:, 0:LANES], v_ref, s, d, dst_off=HALF_SPAN, cast=BF16)

                def scores(b, slot):
                    base = _block_base(b)
                    q = q_rm[pl.ds(base, QBLK), :]
                    zero = jnp.zeros_like(q)
                    q2 = jnp.concatenate([jnp.where(low, q, zero), jnp.where(low, zero, q)], axis=0)
                    sc = lax.dot_general(q2, k_rm[pl.ds(base, KWIN), :], nt_dims, preferred_element_type=F32)
                    bias = band[...] + _range_bias(base, seq)
                    for hh in range(2):
                        rows = slice(hh * QBLK, (hh + 1) * QBLK)
                        sh = sc[rows, :] + bias
                        s_buf[slot, rows, :] = sh
                        m_buf[slot, rows, :] = jnp.broadcast_to(jnp.max(sh, axis=-1, keepdims=True), (QBLK, LANES))

                def outputs(b, slot):
                    base = _block_base(b)
                    sv = s_buf[slot]
                    mb = m_buf[slot]
                    p = jnp.exp(jnp.concatenate([sv[:, 0:LANES] - mb, sv[:, LANES:2 * LANES] - mb], axis=1))
                    pv = jnp.dot(p.astype(BF16), v_rm[pl.ds(base, KWIN), :], preferred_element_type=F32)
                    rows = pl.ds(base, QBLK)
                    acc_rm[rows, :] = jnp.where(low, pv[0:QBLK, 0:LANES], pv[QBLK:2 * QBLK, 0:LANES])
                    l_rm[rows, :] = jnp.where(low, pv[0:QBLK, LANES:2 * LANES], pv[QBLK:2 * QBLK, LANES:2 * LANES])
                    m_rm[rows, :] = jnp.where(low, mb[0:QBLK, :], mb[QBLK:2 * QBLK, :])

                _skewed_blocks(n_blk, scores, outputs)
                if d == 1:
                    src = (acc_rm, m_rm, l_rm)
                else:
                    for dst_, src_ in ((acc_p, acc_rm), (m_p, m_rm), (l_p, l_rm)):
                        _from_residue_major(dst_, src_, s, d)
                    src = (acc_p, m_p, l_p)

                def combine(rows):
                    a_g, m_g, l_g = src[0][rows, :], src[1][rows, :], src[2][rows, :]
                    if gi == 0:
                        m_new, l_new, a_new = m_g, l_g, a_g
                    else:
                        m_old = m_run[rows, :]
                        m_new = jnp.maximum(m_old, m_g)
                        w_old = jnp.exp(m_old - m_new)
                        w_g = jnp.exp(m_g - m_new)
                        l_new = l_run[rows, :] * w_old + l_g * w_g
                        a_new = acc_run[rows, :] * w_old + a_g * w_g
                    if gi == len(DILATIONS) - 1:
                        out = a_new / l_new
                        attn_ref[rows, :] = out
                        attn_b_ref[rows, :] = out.astype(BF16)
                        lse_ref[rows, :] = m_new + jnp.log(l_new)
                    else:
                        m_run[rows, :] = m_new
                        l_run[rows, :] = l_new
                        acc_run[rows, :] = a_new

                _row_chunks(s, combine)

    qk_spec = pl.BlockSpec((s, LANES), lambda hp, g: (0, g * n_pairs + hp))
    v_spec = pl.BlockSpec((s, LANES), lambda hp, g: (0, v_col0 + g * n_pairs + hp))
    o_spec = pl.BlockSpec((s, LANES), lambda hp, g: (0, hp))
    f32buf = pltpu.VMEM((s, LANES), F32)
    return _pallas(
        body, name=name, grid=(n_pairs, len(DILATIONS)), in_specs=[qk_spec, qk_spec, v_spec],
        out_specs=[o_spec, o_spec, o_spec],
        out_shape=[jax.ShapeDtypeStruct((s, n_pairs * LANES), F32)] * 2
        + [jax.ShapeDtypeStruct((s, n_pairs * LANES), BF16)],
        operands=[qn, kn, proj],
        scratch_shapes=[pltpu.VMEM((s, LANES), BF16), pltpu.VMEM((s + 2 * HALF_SPAN, LANES), BF16),
                        pltpu.VMEM((s + 2 * HALF_SPAN, 2 * LANES), BF16)] + [f32buf] * 9
        + [pltpu.VMEM((QBLK, KWIN), F32), pltpu.VMEM((2, 2 * QBLK, KWIN), F32),
           pltpu.VMEM((2, 2 * QBLK, LANES), F32)],
        rider=rider)


def _attn_bwd(qn, kn, proj, dattn, attn, lse, bd, name, rider=None):
    s = qn.shape[0]
    n_pairs = N_SLOT_HEADS * HEAD_DIM // LANES
    v_col0 = 2 * qn.shape[1] // LANES
    nt_dims = (((1,), (1,)), ((), ()))
    tn_dims = (((0,), (0,)), ((), ()))
    spad = s + 2 * HALF_SPAN

    def body(q_ref, k_ref, v_ref, do_ref, o_ref, lse_ref, bd_ref, dq_ref, dk_ref, dv_ref,
             q_rm, k_rm, v_rm, do_rm, lse0_rm, lse1_rm, dd0_rm, dd1_rm, dq_rm, dk_rm, dv_rm,
             lse0_p, lse1_p, dd0_p, dd1_p, band, p_buf, ds_buf):
        g = pl.program_id(1)
        zpad = jnp.zeros((HALF_SPAN, LANES), BF16)
        for buf in (k_rm, v_rm):
            buf[0:HALF_SPAN, :] = zpad
            buf[s + HALF_SPAN: spad, :] = zpad
        zf = jnp.zeros((HALF_SPAN, LANES), F32)
        for buf in (dk_rm, dv_rm):
            buf[0:HALF_SPAN, :] = zf
            buf[s + HALF_SPAN: spad, :] = zf
        band[...] = _band_bias()

        def clear(rows):
            z = jnp.zeros((rows.size, LANES), F32)
            dk_rm[pl.ds(rows.start + HALF_SPAN, rows.size), :] = z
            dv_rm[pl.ds(rows.start + HALF_SPAN, rows.size), :] = z

        _row_chunks(s, clear)

        def prepare(rows):
            lo = lax.broadcasted_iota(jnp.int32, (rows.size, LANES), 1) < HEAD_DIM
            dsum = _head_sums(do_ref[rows, :] * o_ref[rows, :], bd_ref[...])
            dswap = pltpu.roll(dsum, HEAD_DIM, axis=1)
            dd0_p[rows, :] = jnp.where(lo, dsum, dswap)
            dd1_p[rows, :] = jnp.where(lo, dswap, dsum)
            lv = lse_ref[rows, :]
            lswap = pltpu.roll(lv, HEAD_DIM, axis=1)
            lse0_p[rows, :] = jnp.where(lo, lv, lswap)
            lse1_p[rows, :] = jnp.where(lo, lswap, lv)

        @pl.when(g == 0)
        def _():
            _row_chunks(s, prepare)
        lane = lax.broadcasted_iota(jnp.int32, (QBLK, LANES), 1)
        low = lane < HEAD_DIM
        n_blk = s // QBLK

        def stacked(ref, rows):
            val = ref[rows, :]
            zero = jnp.zeros_like(val)
            return jnp.concatenate([jnp.where(low, val, zero), jnp.where(low, zero, val)], axis=0)

        for gi, d in enumerate(DILATIONS):
            @pl.when(g == gi)
            def _(d=d):
                seq = s // d
                _to_residue_major(q_rm, q_ref, s, d, cast=BF16)
                _to_residue_major(k_rm, k_ref, s, d, dst_off=HALF_SPAN, cast=BF16)
                _to_residue_major(v_rm, v_ref, s, d, dst_off=HALF_SPAN, cast=BF16)
                _to_residue_major(do_rm, do_ref, s, d, cast=BF16)
                for dst_, src_ in ((lse0_rm, lse0_p), (lse1_rm, lse1_p), (dd0_rm, dd0_p), (dd1_rm, dd1_p)):
                    _to_residue_major(dst_, src_, s, d)

                def scores(b, slot):
                    base = _block_base(b)
                    rows = pl.ds(base, QBLK)
                    win = pl.ds(base, KWIN)
                    sc = lax.dot_general(stacked(q_rm, rows), k_rm[win, :], nt_dims, preferred_element_type=F32)
                    dp = lax.dot_general(stacked(do_rm, rows), v_rm[win, :], nt_dims, preferred_element_type=F32)
                    bias = band[...] + _range_bias(base, seq)
                    for hh, (lse_r, dd_r) in enumerate(((lse0_rm, dd0_rm), (lse1_rm, dd1_rm))):
                        r = slice(hh * QBLK, (hh + 1) * QBLK)
                        lse_h = lse_r[rows, :]
                        dd_h = dd_r[rows, :]
                        sh = sc[r, :] + bias
                        p = jnp.exp(jnp.concatenate([sh[:, 0:LANES] - lse_h, sh[:, LANES:KWIN] - lse_h], axis=1))
                        dph = dp[r, :]
                        ds = p * jnp.concatenate([dph[:, 0:LANES] - dd_h, dph[:, LANES:KWIN] - dd_h], axis=1)
                        p_buf[slot, r, :] = p.astype(BF16)
                        ds_buf[slot, r, :] = ds.astype(BF16)

                def grads(b, slot):
                    base = _block_base(b)
                    rows = pl.ds(base, QBLK)
                    win = pl.ds(base, KWIN)
                    p = p_buf[slot]
                    ds = ds_buf[slot]
                    dq2 = jnp.dot(ds, k_rm[win, :], preferred_element_type=F32)
                    dq_rm[rows, :] = jnp.where(low, dq2[0:QBLK, :], dq2[QBLK:2 * QBLK, :])
                    dk_rm[win, :] += lax.dot_general(ds, stacked(q_rm, rows), tn_dims, preferred_element_type=F32)
                    dv_rm[win, :] += lax.dot_general(p, stacked(do_rm, rows), tn_dims, preferred_element_type=F32)

                _skewed_blocks(n_blk, scores, grads)
                _from_residue_major(dq_ref, dq_rm, s, d)
                _from_residue_major(dk_ref, dk_rm, s, d, src_off=HALF_SPAN)
                _from_residue_major(dv_ref, dv_rm, s, d, src_off=HALF_SPAN)

    qk_spec = pl.BlockSpec((s, LANES), lambda hp, g: (0, g * n_pairs + hp))
    v_spec = pl.BlockSpec((s, LANES), lambda hp, g: (0, v_col0 + g * n_pairs + hp))
    o_spec = pl.BlockSpec((s, LANES), lambda hp, g: (0, hp))
    width = qn.shape[1]
    f32buf = pltpu.VMEM((s, LANES), F32)
    f32pad = pltpu.VMEM((spad, LANES), F32)
    return _pallas(
        body, name=name, grid=(n_pairs, len(DILATIONS)),
        in_specs=[qk_spec, qk_spec, v_spec, o_spec, o_spec, o_spec,
                  pl.BlockSpec((LANES, LANES), lambda hp, g: (0, 0))],
        out_specs=[qk_spec, qk_spec, qk_spec],
        out_shape=[jax.ShapeDtypeStruct((s, width), F32)] * 3,
        operands=[qn, kn, proj, dattn, attn, lse, bd],
        scratch_shapes=[pltpu.VMEM((s, LANES), BF16), pltpu.VMEM((spad, LANES), BF16),
                        pltpu.VMEM((spad, LANES), BF16), pltpu.VMEM((s, LANES), BF16),
                        f32buf, f32buf, f32buf, f32buf, f32buf, f32pad, f32pad,
                        f32buf, f32buf, f32buf, f32buf, pltpu.VMEM((QBLK, KWIN), F32),
                        pltpu.VMEM((2, 2 * QBLK, KWIN), BF16), pltpu.VMEM((2, 2 * QBLK, KWIN), BF16)],
        rider=rider)


CONV_PAD = 16


def _conv_fwd(proj, conv_w, conv_b, col0, name, rider=None):
    s = proj.shape[0]
    ch = conv_w.shape[1]
    nblk = ch // LANES
    a0 = col0 // LANES
    tr = 256
    shift = CONV_PAD - (CONV_WIDTH - 1) // 2

    def body(a_ref, b_ref, w_ref, bias_ref, u0_ref, uc_ref, pad):
        z = jnp.zeros((CONV_PAD, LANES), F32)
        pad[0:CONV_PAD, :] = z
        pad[s + CONV_PAD: s + 2 * CONV_PAD, :] = z

        def glu(rows):
            u0 = a_ref[rows, :] * jax.nn.sigmoid(b_ref[rows, :])
            u0_ref[rows, :] = u0
            pad[pl.ds(rows.start + CONV_PAD, rows.size), :] = u0

        _row_chunks(s, glu)
        for t in range(0, s, tr):
            acc = jnp.broadcast_to(bias_ref[...], (tr, LANES))
            for k in range(CONV_WIDTH):
                acc = acc + w_ref[k:k + 1, :] * pad[t + k + shift: t + k + shift + tr, :]
            uc_ref[t:t + tr, :] = acc

    return _pallas(
        body, name=name, grid=(nblk,),
        in_specs=[pl.BlockSpec((s, LANES), lambda c: (0, a0 + c)),
                  pl.BlockSpec((s, LANES), lambda c: (0, a0 + nblk + c)),
                  pl.BlockSpec((CONV_WIDTH, LANES), lambda c: (0, c)),
                  pl.BlockSpec((1, LANES), lambda c: (0, c))],
        out_specs=[pl.BlockSpec((s, LANES), lambda c: (0, c))] * 2,
        out_shape=[jax.ShapeDtypeStruct((s, ch), F32)] * 2, operands=[proj, proj, conv_w, conv_b],
        scratch_shapes=[pltpu.VMEM((s + 2 * CONV_PAD, LANES), F32)], rider=rider)


def _ln_silu_fwd(uc, ln_w, ln_b, name):
    s, ch = uc.shape
    tm = 256

    def body(u_ref, w_ref, b_ref, o_ref):
        u = u_ref[...]
        mu = jnp.mean(u, axis=-1, keepdims=True)
        xc = u - mu
        rstd = lax.rsqrt(jnp.mean(xc * xc, axis=-1, keepdims=True) + EPS)
        z = xc * rstd * w_ref[...] + b_ref[...]
        o_ref[...] = (z * jax.nn.sigmoid(z)).astype(BF16)

    row = pl.BlockSpec((tm, ch), lambda i: (i, 0))
    vec = pl.BlockSpec((1, ch), lambda i: (0, 0))
    return pl.pallas_call(
        body, name=name, grid=(s // tm,), in_specs=[row, vec, vec], out_specs=row,
        out_shape=jax.ShapeDtypeStruct((s, ch), BF16), compiler_params=_params(),
    )(uc, ln_w, ln_b)


def _ln_silu_bwd(du3, uc, ln_w, ln_b, name):
    s, ch = uc.shape
    tm = 256

    def body(d_ref, u_ref, w_ref, b_ref, du_ref, dw_ref, db_ref):
        u = u_ref[...]
        mu = jnp.mean(u, axis=-1, keepdims=True)
        xc = u - mu
        rstd = lax.rsqrt(jnp.mean(xc * xc, axis=-1, keepdims=True) + EPS)
        xhat = xc * rstd
        z = xhat * w_ref[...] + b_ref[...]
        sg = jax.nn.sigmoid(z)
        dz = d_ref[...] * (sg * (1.0 + z * (1.0 - sg)))
        dxh = dz * w_ref[...]
        du_ref[...] = rstd * (dxh - jnp.mean(dxh, axis=-1, keepdims=True)
                              - xhat * jnp.mean(dxh * xhat, axis=-1, keepdims=True))
        pw = jnp.sum(dz * xhat, axis=0, keepdims=True)
        pb = jnp.sum(dz, axis=0, keepdims=True)
        first = pl.program_id(0) == 0

        @pl.when(first)
        def _():
            dw_ref[...] = pw
            db_ref[...] = pb

        @pl.when(jnp.logical_not(first))
        def _():
            dw_ref[...] += pw
            db_ref[...] += pb

    row = pl.BlockSpec((tm, ch), lambda i: (i, 0))
    vec = pl.BlockSpec((1, ch), lambda i: (0, 0))
    return pl.pallas_call(
        body, name=name, grid=(s // tm,), in_specs=[row, row, vec, vec], out_specs=[row, vec, vec],
        out_shape=[jax.ShapeDtypeStruct((s, ch), F32), jax.ShapeDtypeStruct((1, ch), F32),
                   jax.ShapeDtypeStruct((1, ch), F32)],
        compiler_params=_params(),
    )(du3, uc, ln_w, ln_b)


def _conv_bwd(duc, u0, proj, conv_w, col0, name, rider=None):
    s = proj.shape[0]
    ch = conv_w.shape[1]
    nblk = ch // LANES
    a0 = col0 // LANES
    tr = 256
    half = (CONV_WIDTH - 1) // 2
    shift = CONV_PAD - half

    def body(duc_ref, u0_ref, a_ref, b_ref, w_ref, da_ref, db_ref, dw_ref, dbias_ref, pad_d, pad_u):
        z = jnp.zeros((CONV_PAD, LANES), F32)
        for buf in (pad_d, pad_u):
            buf[0:CONV_PAD, :] = z
            buf[s + CONV_PAD: s + 2 * CONV_PAD, :] = z

        def fill(rows):
            dst = pl.ds(rows.start + CONV_PAD, rows.size)
            pad_d[dst, :] = duc_ref[rows, :]
            pad_u[dst, :] = u0_ref[rows, :]

        _row_chunks(s, fill)
        dw_acc = [jnp.zeros((8, LANES), F32) for _ in range(CONV_WIDTH)]
        dbias_acc = jnp.zeros((8, LANES), F32)
        for t in range(0, s, tr):
            d_t = duc_ref[t:t + tr, :]
            dbias_acc = dbias_acc + jnp.sum(d_t.reshape(tr // 8, 8, LANES), axis=0)
            du0 = jnp.zeros((tr, LANES), F32)
            for k in range(CONV_WIDTH):
                du0 = du0 + w_ref[k:k + 1, :] * pad_d[t - k + half + CONV_PAD: t - k + half + CONV_PAD + tr, :]
                prod = d_t * pad_u[t + k + shift: t + k + shift + tr, :]
                dw_acc[k] = dw_acc[k] + jnp.sum(prod.reshape(tr // 8, 8, LANES), axis=0)
            av = a_ref[t:t + tr, :]
            sg = jax.nn.sigmoid(b_ref[t:t + tr, :])
            da_ref[t:t + tr, :] = (du0 * sg).astype(BF16)
            db_ref[t:t + tr, :] = (du0 * av * sg * (1.0 - sg)).astype(BF16)
        for k in range(CONV_WIDTH):
            dw_ref[k:k + 1, :] = jnp.sum(dw_acc[k], axis=0, keepdims=True)
        dbias_ref[...] = jnp.sum(dbias_acc, axis=0, keepdims=True)

    col = lambda off: pl.BlockSpec((s, LANES), lambda c: (0, off + c))
    return _pallas(
        body, name=name, grid=(nblk,),
        in_specs=[col(0), col(0), col(a0), col(a0 + nblk),
                  pl.BlockSpec((CONV_WIDTH, LANES), lambda c: (0, c))],
        out_specs=[col(0), col(0), pl.BlockSpec((CONV_WIDTH, LANES), lambda c: (0, c)),
                   pl.BlockSpec((1, LANES), lambda c: (0, c))],
        out_shape=[jax.ShapeDtypeStruct((s, ch), BF16)] * 2
        + [jax.ShapeDtypeStruct((CONV_WIDTH, ch), F32), jax.ShapeDtypeStruct((1, ch), F32)],
        operands=[duc, u0, proj, proj, conv_w],
        scratch_shapes=[pltpu.VMEM((s + 2 * CONV_PAD, LANES), F32)] * 2, rider=rider)


def _mix_out_proj(attn_b, u3, w_o, w_pw, proj, bg, col0, w_out, x, norm_w, name, rider=None):
    s, d = x.shape
    k = attn_b.shape[1]
    tm = 256
    half = d // 2
    assert col0 % half == 0
    c0 = col0 // half

    def body(a_ref, u_ref, wo_ref, wp_ref, a0_ref, a1_ref, b0_ref, b1_ref, bias_ref, w_ref, x_ref, nw_ref,
             ya_ref, yb_ref, mixed_ref, x1_ref, h2_ref):
        mixed = None
        for br, (src_ref, wb_ref, lo_ref, hi_ref, y_ref) in enumerate(((a_ref, wo_ref, a0_ref, a1_ref, ya_ref),
                                                                       (u_ref, wp_ref, b0_ref, b1_ref, yb_ref))):
            src = src_ref[...]
            y = jnp.concatenate([jnp.dot(src, wb_ref[j], preferred_element_type=F32) for j in range(N_CHIPS)],
                                axis=1)
            y_ref[...] = y
            logits = jnp.concatenate([lo_ref[...], hi_ref[...]], axis=1)
            part = jax.nn.sigmoid(logits + bias_ref[br]) * y
            mixed = part if mixed is None else mixed + part
        mixed = mixed.astype(BF16)
        mixed_ref[...] = mixed
        x1 = x_ref[...] + jnp.dot(mixed, w_ref[...], preferred_element_type=F32)
        x1_ref[...] = x1
        rstd = lax.rsqrt(jnp.mean(x1 * x1, axis=-1, keepdims=True) + EPS)
        h2_ref[...] = (x1 * rstd * nw_ref[...]).astype(BF16)

    row = pl.BlockSpec((tm, d), lambda i: (i, 0))
    src_row = pl.BlockSpec((tm, k), lambda i: (i, 0))
    blocks = pl.BlockSpec(w_o.shape, lambda i: (0, 0, 0))
    logit_blk = lambda j: pl.BlockSpec((tm, half), functools.partial(lambda i, j: (i, c0 + j), j=j))
    return _pallas(
        body, name=name, grid=(s // tm,),
        in_specs=[src_row, src_row, blocks, blocks, logit_blk(0), logit_blk(1), logit_blk(2), logit_blk(3),
                  pl.BlockSpec((2, 1, d), lambda i: (0, 0, 0)), pl.BlockSpec((d, d), lambda i: (0, 0)), row,
                  pl.BlockSpec((1, d), lambda i: (0, 0))],
        out_specs=[row] * 5,
        out_shape=[jax.ShapeDtypeStruct((s, d), F32), jax.ShapeDtypeStruct((s, d), F32),
                   jax.ShapeDtypeStruct((s, d), BF16), jax.ShapeDtypeStruct((s, d), F32),
                   jax.ShapeDtypeStruct((s, d), BF16)],
        operands=[attn_b, u3, w_o, w_pw, proj, proj, proj, proj, bg, w_out, x, norm_w], rider=rider)


def _out_proj_bwd_gates(dx1, w_out, proj, bg, y_a, y_b, col0, name, after=()):
    s, d = y_a.shape
    tm = 256
    half = d // 2
    assert col0 % half == 0
    c0 = col0 // half
    nt_dims = (((1,), (1,)), ((), ()))

    def body(dx_ref, w_ref, a0_ref, a1_ref, b0_ref, b1_ref, bias_ref, ya_ref, yb_ref,
             dgl_ref, dya_ref, dyb_ref, db_ref):
        dm = lax.dot_general(dx_ref[...], w_ref[...], nt_dims, preferred_element_type=F32)
        parts = []
        for br, (lo_ref, hi_ref, y_ref, dy_ref) in enumerate(((a0_ref, a1_ref, ya_ref, dya_ref),
                                                              (b0_ref, b1_ref, yb_ref, dyb_ref))):
            logits = jnp.concatenate([lo_ref[...], hi_ref[...]], axis=1)
            gate = jax.nn.sigmoid(logits + bias_ref[br])
            dy_ref[...] = (dm * gate).astype(BF16)
            dgl = dm * y_ref[...] * gate * (1.0 - gate)
            dgl_ref[:, br * d:(br + 1) * d] = dgl.astype(BF16)
            parts.append(jnp.sum(dgl, axis=0, keepdims=True))
        part = jnp.concatenate(parts, axis=0)
        first = pl.program_id(0) == 0

        @pl.when(first)
        def _():
            db_ref[...] = part

        @pl.when(jnp.logical_not(first))
        def _():
            db_ref[...] += part

    row = pl.BlockSpec((tm, d), lambda i: (i, 0))
    logit_blk = lambda k: pl.BlockSpec((tm, half), functools.partial(lambda i, k: (i, c0 + k), k=k))
    return _pallas(
        body, name=name, grid=(s // tm,),
        in_specs=[row, pl.BlockSpec((d, d), lambda i: (0, 0)), logit_blk(0), logit_blk(1), logit_blk(2),
                  logit_blk(3), pl.BlockSpec((2, 1, d), lambda i: (0, 0, 0)), row, row],
        out_specs=[pl.BlockSpec((tm, 2 * d), lambda i: (i, 0)), row, row, pl.BlockSpec((2, d), lambda i: (0, 0))],
        out_shape=[jax.ShapeDtypeStruct((s, 2 * d), BF16), jax.ShapeDtypeStruct((s, d), BF16),
                   jax.ShapeDtypeStruct((s, d), BF16), jax.ShapeDtypeStruct((2, d), F32)],
        operands=[dx1, w_out, proj, proj, proj, proj, bg, y_a, y_b], after=after)


def _ffn_in_swiglu(h2, w_blocked, name):
    s, k = h2.shape
    nblk, _, tn = w_blocked.shape
    ff = nblk // 2 * tn
    tm = 512

    def body(a_ref, wg_ref, wu_ref, g_ref, u_ref, act_ref):
        a = a_ref[...]
        gt = jnp.dot(a, wg_ref[...], preferred_element_type=F32)
        up = jnp.dot(a, wu_ref[...], preferred_element_type=F32)
        g_ref[...] = gt
        u_ref[...] = up
        act_ref[...] = (gt * jax.nn.sigmoid(gt) * up).astype(BF16)

    out = pl.BlockSpec((tm, tn), lambda j, i: (i, j))
    return pl.pallas_call(
        body, name=name, grid=(nblk // 2, s // tm),
        in_specs=[pl.BlockSpec((tm, k), lambda j, i: (i, 0)),
                  pl.BlockSpec((None, k, tn), lambda j, i: (j, 0, 0)),
                  pl.BlockSpec((None, k, tn), lambda j, i: (nblk // 2 + j, 0, 0))],
        out_specs=[out, out, out],
        out_shape=[jax.ShapeDtypeStruct((s, ff), F32), jax.ShapeDtypeStruct((s, ff), F32),
                   jax.ShapeDtypeStruct((s, ff), BF16)],
        compiler_params=_params(),
    )(h2, w_blocked, w_blocked)


def _ffn_out_bwd_swiglu(dy, w_ffn_out, gate, up, name, rider=None):
    s, d = dy.shape
    ff = gate.shape[1]
    tm = 256
    nt_dims = (((1,), (1,)), ((), ()))

    def body(dy_ref, w_ref, g_ref, u_ref, o_ref):
        dv = lax.dot_general(dy_ref[...], w_ref[...], nt_dims, preferred_element_type=F32)
        gt = g_ref[...]
        sg = jax.nn.sigmoid(gt)
        o_ref[:, 0:ff] = (dv * u_ref[...] * (sg * (1.0 + gt * (1.0 - sg)))).astype(BF16)
        o_ref[:, ff:2 * ff] = (dv * gt * sg).astype(BF16)

    row = pl.BlockSpec((tm, ff), lambda i: (i, 0))
    return _pallas(
        body, name=name, grid=(s // tm,),
        in_specs=[pl.BlockSpec((tm, d), lambda i: (i, 0)), pl.BlockSpec((ff, d), lambda i: (0, 0)), row, row],
        out_specs=pl.BlockSpec((tm, 2 * ff), lambda i: (i, 0)),
        out_shape=jax.ShapeDtypeStruct((s, 2 * ff), BF16), operands=[dy, w_ffn_out, gate, up], rider=rider)


def _ffn_out_loss(act, w_ffn_out, x1, target, name):
    s, k = act.shape
    d = w_ffn_out.shape[1]
    tm = 512

    def body(a_ref, w_ref, x1_ref, t_ref, dy_ref, dyb_ref, loss_ref, acc):
        y = x1_ref[...] + jnp.dot(a_ref[...], w_ref[...], preferred_element_type=F32)
        diff = y - t_ref[...]
        dy = diff * (1.0 / d)
        dy_ref[...] = dy
        dyb_ref[...] = dy.astype(BF16)
        part = jnp.sum((diff * diff).reshape(tm // 8, 8, d), axis=0)
        i = pl.program_id(0)

        @pl.when(i == 0)
        def _():
            acc[...] = part

        @pl.when(i > 0)
        def _():
            acc[...] += part

        @pl.when(i == pl.num_programs(0) - 1)
        def _():
            loss_ref[...] = (0.5 / d) * jnp.sum(jnp.sum(acc[...], axis=1, keepdims=True), axis=0, keepdims=True)

    row = pl.BlockSpec((tm, d), lambda i: (i, 0))
    return pl.pallas_call(
        body, name=name, grid=(s // tm,),
        in_specs=[pl.BlockSpec((tm, k), lambda i: (i, 0)), pl.BlockSpec((k, d), lambda i: (0, 0)), row, row],
        out_specs=[row, row, pl.BlockSpec((1, 1), lambda i: (0, 0))],
        out_shape=[jax.ShapeDtypeStruct((s, d), F32), jax.ShapeDtypeStruct((s, d), BF16),
                   jax.ShapeDtypeStruct((1, 1), F32)],
        scratch_shapes=[pltpu.VMEM((8, d), F32)], compiler_params=_params(),
    )(act, w_ffn_out, x1, target)


LATE_GATHER = ("w_o_attn", "w_pw_conv", "w_out", "w_ffn_in", "w_ffn_out")
EARLY_REDUCE = LATE_GATHER


def _blocks_by_half(g):
    if g.ndim == 2:
        g = g.reshape(N_CHIPS, g.shape[0] // N_CHIPS, g.shape[1])
    return g.reshape(N_CHIPS, 2, g.shape[1] // 2, g.shape[2])


def _forward_backward(x, pos_col, target, wts, first_gather, late_bufs, pos_arr):
    wts = dict(wts)
    consts = _rope_consts()
    bd = consts[3]
    qw2 = jnp.tile(wts["q_norm_w"], (1, LANES // HEAD_DIM))
    kw2 = jnp.tile(wts["k_norm_w"], (1, LANES // HEAD_DIM))
    qkv_w = 3 * N_SLOT_HEADS * HEAD_DIM
    conv_col0 = 3 * qkv_w

    h = _rmsnorm_fwd(x, wts["norm1_w"], "rms1_fwd")
    slot_order = jnp.bitwise_xor(pos_arr[1], jnp.asarray([0, 2, 1, 3], jnp.int32))
    blocked = lambda buf: buf.reshape(N_CHIPS, -1, buf.shape[3])
    near = first_gather
    proj = _proj_by_slot(h, blocked(near[2][0]), slot_order, 0, 1, "mm_proj_own", after=list(late_bufs))
    near = _split_middle(near, after=[proj], name="allgather_w_in_near_forward")
    far, _ = _split_start(_gather_both_legs_rider(near[2][:1], near[2][1:], FAR_CHIPS), "allgather_w_in_far_start")
    _, bufs = _split_wait((near[0], near[1], far[2], near[3]), after=[], name="allgather_w_in_near_wait")
    proj = _proj_by_slot(h, blocked(bufs[0]), slot_order, 1, len(NEAR_CHIPS), "mm_proj_near", proj=proj)
    far = _split_middle((far[0], far[1], bufs, far[3]), after=[proj], name="allgather_w_in_far_forward")
    (w_in_buf, conv_w_buf, b_gate_buf), _ = _split_wait(far, after=[], name="allgather_w_in_far_wait")
    wts["w_in"] = w_in_buf.reshape(N_CHIPS, -1, w_in_buf.shape[3])
    wts["conv_w"] = conv_w_buf.transpose(1, 0, 2).reshape(CONV_WIDTH, -1)
    wts["b_gate"] = b_gate_buf.transpose(1, 0, 2).reshape(2, 1, -1)
    ch = wts["conv_w"].shape[1]
    gate_col0 = conv_col0 + 2 * ch
    n_mix = LATE_GATHER.index("w_ffn_in")
    mix_gather, started = _split_start(_gather_ici_rider(late_bufs[:n_mix], []), "late_gather_mix_start",
                                       after=[wts["w_in"]])
    ffn_gather, started = _split_start(_gather_ici_rider(late_bufs[n_mix:], []), "late_gather_ffn_start",
                                       after=[started])
    proj = _proj_by_slot(h, wts["w_in"], slot_order, 1 + len(NEAR_CHIPS), len(FAR_CHIPS), "mm_proj_far", proj=proj,
                         after=[started])
    qn, kn = _qk_fwd(proj, pos_col, qw2, kw2, consts, "qk_fwd")
    attn, lse, attn_b = _attn_fwd(qn, kn, proj, "attn_fwd")

    def gathered(names, bufs):
        for n, buf in zip(names, bufs):
            full = buf.reshape(N_CHIPS, -1, buf.shape[3])
            wts[n] = full.reshape(-1, full.shape[2]) if n in ROW_SHARDED else full

    mix_bufs, _ = _split_wait(mix_gather, after=[attn_b], name="late_gather_mix_wait")
    (u0, uc), mix_bufs = _conv_fwd(proj, wts["conv_w"], wts["conv_b"], conv_col0, "conv_fwd",
                                   rider=_gather_forward_rider(mix_bufs))
    gathered(LATE_GATHER[:n_mix], mix_bufs)
    u3 = _ln_silu_fwd(uc, wts["conv_ln_w"], wts["conv_ln_b"], "ln_fwd")
    ffn_bufs, _ = _split_wait(ffn_gather, after=[u3], name="late_gather_ffn_wait")
    (y_a, y_b, mixed, x1, h2), ffn_bufs = _mix_out_proj(
        attn_b, u3, wts["w_o_attn"], wts["w_pw_conv"], proj, wts["b_gate"], gate_col0, wts["w_out"], x,
        wts["norm2_w"], "mix_x1_rms2", rider=_gather_forward_rider(ffn_bufs))
    gathered(LATE_GATHER[n_mix:], ffn_bufs)
    gate, up, act = _ffn_in_swiglu(h2, wts["w_ffn_in"], "mm_gu_swiglu")
    dy, dy_b16, loss = _ffn_out_loss(act, wts["w_ffn_out"], x1, target, "mm_x2_loss")

    g = {}
    by_chip = {}

    def pair_add(n, blocks, received):
        return _add_own_half(blocks, received, pos_arr, f"grads_pair_add_{n}")

    g_ffn_out = _blocks_by_half(
        _matmul(act, dy_b16, mode="tn", tm=1408, tn=1024, tk=2048, out_dtype=F32, name="mm_dwffnout"))
    dgu, (received,) = _ffn_out_bwd_swiglu(dy_b16, wts["w_ffn_out"], gate, up, "mm_dact_swiglu_bwd",
                                           rider=_pair_exchange_rider([g_ffn_out], halved=True))
    to_send, own = pair_add("w_ffn_out", g_ffn_out, received)
    dh2, (by_chip["w_ffn_out"],) = _matmul(
        dgu, wts["w_ffn_in"], mode="nt", tm=1024, tn=1024, tk=1408, out_dtype=F32, name="mm_dh2", b_blocked=True,
        rider=_chip_exchange_rider([to_send], [own]))
    g_ffn_in = _blocks_by_half(_matmul(h2, dgu, mode="tn", tm=512, tn=1408, tk=2048, out_dtype=F32,
                                       name="mm_dwffnin", out_blocked=N_CHIPS, cols_outer=True))
    exchanging, started = _split_start(_pair_exchange_rider([g_ffn_in], halved=True), "grads_ffn_in_pair_start")
    dx1, dx1_b16, g["norm2_w"] = _rmsnorm_bwd(dh2, x1, wts["norm2_w"], dy, "rms2_bwd")
    g["w_out"] = _matmul(mixed, dx1_b16, mode="tn", tm=512, tn=1024, tk=2048, out_dtype=F32, name="mm_dwout",
                         after=[started])
    dgl, dy_a, dy_b, g["b_gate"] = _out_proj_bwd_gates(dx1_b16, wts["w_out"], proj, wts["b_gate"], y_a, y_b,
                                                       gate_col0, "mm_dmixed_gate_bwd")
    (received,), (g_ffn_in,) = _split_wait(exchanging, after=[dgl], name="grads_ffn_in_pair_wait")
    ffn_in_to_send, ffn_in_own = pair_add("w_ffn_in", g_ffn_in, received)
    dattn = _matmul(dy_a, wts["w_o_attn"], mode="nt", tm=1024, tn=512, tk=256, out_dtype=F32, name="mm_dattn",
                    b_blocked=True)
    g["w_o_attn"] = _matmul(attn_b, dy_a, mode="tn", tm=512, tn=256, tk=2048, out_dtype=F32, name="mm_dwo",
                            out_blocked=N_CHIPS)
    du3 = _matmul(dy_b, wts["w_pw_conv"], mode="nt", tm=1024, tn=512, tk=256, out_dtype=F32, name="mm_du3",
                  b_blocked=True)
    g["w_pw_conv"] = _matmul(u3, dy_b, mode="tn", tm=512, tn=256, tk=2048, out_dtype=F32, name="mm_dwpw",
                             out_blocked=N_CHIPS)
    duc, g["conv_ln_w"], g["conv_ln_b"] = _ln_silu_bwd(du3, uc, wts["conv_ln_w"], wts["conv_ln_b"], "ln_bwd")

    small3 = ("w_out", "w_o_attn", "w_pw_conv")
    g_small3 = [_blocks_by_half(g.pop(n)) for n in small3]
    (da, db, g["conv_w"], g["conv_b"]), received = _conv_bwd(
        duc, u0, proj, wts["conv_w"], conv_col0, "conv_bwd", rider=_pair_exchange_rider(g_small3, halved=True))
    sums3 = [pair_add(n, gb, rv) for n, gb, rv in zip(small3, g_small3, received)]
    (dqn, dkn, dv), (by_chip["w_ffn_in"],) = _attn_bwd(
        qn, kn, proj, dattn, attn, lse, bd, "attn_bwd",
        rider=_chip_exchange_rider([ffn_in_to_send], [ffn_in_own]))
    (dproj, dqw, dkw), exchanged3 = _qk_bwd(
        dqn, dkn, dv, da, db, dgl, proj, pos_col, qw2, kw2, consts, "qk_bwd",
        rider=_chip_exchange_rider([s[0] for s in sums3], [s[1] for s in sums3]))
    by_chip.update(zip(small3, exchanged3))
    halves = [_sum_chips(by_chip[n], pos_arr, f"grads_chip_sum_{n}") for n in EARLY_REDUCE]
    g["q_norm_w"] = dqw[:, :HEAD_DIM]
    g["k_norm_w"] = dkw[:, :HEAD_DIM]

    c = pos_arr[0]
    rh = h.shape[1] // 2
    h_sibling = lax.dynamic_slice_in_dim(h, (1 - c) * rh, rh, axis=1)
    h_own = lax.dynamic_slice_in_dim(h, c * rh, rh, axis=1)
    g_sibling, shards = _matmul(h_sibling, dproj, mode="tn", tm=rh, tn=1920, tk=2048, out_dtype=F32,
                                name="mm_dwin_sibling", out_blocked=N_CHIPS, rider=_pair_gather_rider(halves))
    reduced = dict(zip(EARLY_REDUCE, shards))
    exchanging, started = _split_start(_pair_exchange_rider([g_sibling], halved=False), "grads_w_in_pair_start")
    g_own = _matmul(h_own, dproj, mode="tn", tm=rh, tn=1920, tk=2048, out_dtype=F32, name="mm_dwin_own",
                    out_blocked=N_CHIPS, after=[started])
    (from_sibling,), _ = _split_wait(exchanging, after=[g_own], name="grads_w_in_pair_wait")
    to_send, own = _add_own_half(g_own, from_sibling, pos_arr, "grads_pair_add_w_in")
    in_flight, started = _split_start(_chip_exchange_rider([to_send], [own]), "grads_w_in_exchange_start")
    dh = _matmul(dproj, wts["w_in"], mode="nt", tm=1024, tn=1024, tk=1920, out_dtype=F32, name="mm_dh",
                 b_blocked=True, after=[started])
    grad_x, _, g["norm1_w"] = _rmsnorm_bwd(dh, x, wts["norm1_w"], dx1, "rms1_bwd")
    return loss, grad_x, g, reduced, (in_flight, started)


def _mesh_pos():
    return lax.axis_index("x"), lax.axis_index("y"), lax.axis_index("c")


def _other_chips(x, y):
    return [(1 - x, y), (x, 1 - y), (1 - x, 1 - y)]


NEAR_CHIPS, FAR_CHIPS = (0, 1), (2,)


def _cast_into_slot(shard, chip_arr, dtype, name, n_slots=N_CHIPS, after=()):
    r, c = shard.shape
    tr = r // 2 if r % 32 == 0 else r

    def body(chip_ref, s_ref, *refs):
        refs[-1][...] = s_ref[...].astype(dtype)

    return pl.pallas_call(
        body, name=name,
        grid_spec=pltpu.PrefetchScalarGridSpec(
            num_scalar_prefetch=1, grid=(r // tr,),
            in_specs=[pl.BlockSpec((tr, c), lambda i, chip_ref: (i, 0))] + [ANY] * len(after),
            out_specs=pl.BlockSpec((None, tr, c), lambda i, chip_ref: (chip_ref[0], i, 0))),
        out_shape=jax.ShapeDtypeStruct((n_slots, r, c), dtype), compiler_params=_params(),
    )(chip_arr, shard, *after)


GATHER_CHUNKS = 4


def _gather_both_legs_rider(big, small, peers):
    nb = len(big)
    n = nb + len(small)
    nch = GATHER_CHUNKS

    def part(bufs, a, slot, half, ch):
        if a >= nb:
            return bufs[a].at[slot]
        rows = bufs[a].shape[2] // nch
        return bufs[a].at[slot, half, pl.ds(ch * rows, rows)]

    def pieces():
        return [(a, ch, k) for ch in range(nch) for a in range(n) for k in peers if a < nb or ch == 0]

    def ici(bufs, sems, a, ch, k, slot_of_src):
        x, y, c = _mesh_pos()
        px, py = _other_chips(x, y)[k]
        slot = 2 * x + y if slot_of_src == "mine" else 2 * px + py
        return pltpu.make_async_remote_copy(
            src_ref=part(bufs, a, slot, c, ch), dst_ref=part(bufs, a, slot, c, ch), send_sem=sems[0].at[a, ch, k],
            recv_sem=sems[1].at[a, ch, k], device_id=(px, py, c), device_id_type=MESH)

    def forward(bufs, sems, a, ch, k, half):
        x, y, c = _mesh_pos()
        px, py = _other_chips(x, y)[k]
        h = c if half == "mine" else 1 - c
        return pltpu.make_async_remote_copy(
            src_ref=part(bufs, a, 2 * px + py, h, ch), dst_ref=part(bufs, a, 2 * px + py, h, ch),
            send_sem=sems[2].at[a, ch, k], recv_sem=sems[3].at[a, ch, k], device_id=(x, y, 1 - c),
            device_id_type=MESH)

    def start(r_in, bufs, sems):
        for a, ch, k in pieces():
            ici(bufs, sems, a, ch, k, "mine").start()

    def middle(r_in, bufs, sems):
        for a, ch, k in pieces():
            ici(bufs, sems, a, ch, k, "theirs").wait_recv()
            if a < nb:
                forward(bufs, sems, a, ch, k, "mine").start()
        for a, ch, k in pieces():
            ici(bufs, sems, a, ch, k, "mine").wait_send()

    def wait(r_in, bufs, sems):
        for a, ch, k in pieces():
            if a < nb:
                forward(bufs, sems, a, ch, k, "theirs").wait_recv()
        for a, ch, k in pieces():
            if a < nb:
                forward(bufs, sems, a, ch, k, "mine").wait_send()

    ops = list(big) + list(small)
    return _Rider(ops, [jax.ShapeDtypeStruct(o.shape, o.dtype) for o in ops], {i: i for i in range(n)},
                  [pltpu.SemaphoreType.DMA((n, nch, 3)), pltpu.SemaphoreType.DMA((n, nch, 3)),
                   pltpu.SemaphoreType.DMA((nb, nch, 3)), pltpu.SemaphoreType.DMA((nb, nch, 3))],
                  start, wait, middle)


def _comm_call(rider, name):
    def body():
        pass

    return _pallas(body, name=name, grid=(1,), in_specs=[], out_specs=[], out_shape=[], operands=[],
                   rider=rider)[1]


def _gather_ici_rider(big, small):
    nb = len(big)
    n = nb + len(small)

    def copies(bufs, sems):
        x, y, c = _mesh_pos()
        me = 2 * x + y
        part = lambda a, slot: bufs[a].at[slot, c] if a < nb else bufs[a].at[slot]
        out = []
        for a in range(n):
            for k, (px, py) in enumerate(_other_chips(x, y)):
                send = functools.partial(
                    pltpu.make_async_remote_copy,
                    src_ref=part(a, me), dst_ref=part(a, me), send_sem=sems[0].at[a, k],
                    recv_sem=sems[1].at[a, k], device_id=(px, py, c), device_id_type=MESH)
                recv = functools.partial(
                    pltpu.make_async_remote_copy,
                    src_ref=part(a, 2 * px + py), dst_ref=part(a, 2 * px + py), send_sem=sems[0].at[a, k],
                    recv_sem=sems[1].at[a, k], device_id=(px, py, c), device_id_type=MESH)
                out.append((send, recv))
        return out

    def start(r_in, r_out, sems):
        for send, _ in copies(r_out, sems):
            send().start()

    def wait(r_in, r_out, sems):
        cps = copies(r_out, sems)
        for _, recv in cps:
            recv().wait_recv()
        for send, _ in cps:
            send().wait_send()

    ops = list(big) + list(small)
    return _Rider(ops, [jax.ShapeDtypeStruct(o.shape, o.dtype) for o in ops], {i: i for i in range(n)},
                  [pltpu.SemaphoreType.DMA((n, 3)), pltpu.SemaphoreType.DMA((n, 3))], start, wait)


def _gather_forward_rider(big):
    n = len(big)

    def copies(bufs, sems):
        x, y, c = _mesh_pos()
        out = []
        for a in range(n):
            for k, (px, py) in enumerate(_other_chips(x, y)):
                slot = 2 * px + py
                send = functools.partial(
                    pltpu.make_async_remote_copy,
                    src_ref=bufs[a].at[slot, c], dst_ref=bufs[a].at[slot, c], send_sem=sems[0].at[a, k],
                    recv_sem=sems[1].at[a, k], device_id=(x, y, 1 - c), device_id_type=MESH)
                recv = functools.partial(
                    pltpu.make_async_remote_copy,
                    src_ref=bufs[a].at[slot, 1 - c], dst_ref=bufs[a].at[slot, 1 - c], send_sem=sems[0].at[a, k],
                    recv_sem=sems[1].at[a, k], device_id=(x, y, 1 - c), device_id_type=MESH)
                out.append((send, recv))
        return out

    def start(r_in, r_out, sems):
        for send, _ in copies(r_out, sems):
            send().start()

    def wait(r_in, r_out, sems):
        cps = copies(r_out, sems)
        for _, recv in cps:
            recv().wait_recv()
        for send, _ in cps:
            send().wait_send()

    return _Rider(big, [jax.ShapeDtypeStruct(o.shape, o.dtype) for o in big], {i: i for i in range(n)},
                  [pltpu.SemaphoreType.DMA((n, 3)), pltpu.SemaphoreType.DMA((n, 3))], start, wait)


def _pair_exchange_rider(gs, halved):
    n = len(gs)

    def copies(r_in, r_out, sems):
        x, y, c = _mesh_pos()
        return [pltpu.make_async_remote_copy(
            src_ref=r_in[a].at[:, 1 - c] if halved else r_in[a], dst_ref=r_out[a], send_sem=sems[0].at[a],
            recv_sem=sems[1].at[a], device_id=(x, y, 1 - c), device_id_type=MESH) for a in range(n)]

    def start(r_in, r_out, sems):
        for cp in copies(r_in, r_out, sems):
            cp.start()

    def wait(r_in, r_out, sems):
        for cp in copies(r_in, r_out, sems):
            cp.wait()

    return _Rider(gs, [jax.ShapeDtypeStruct((g.shape[0],) + g.shape[-2:], g.dtype) for g in gs], {},
                  [pltpu.SemaphoreType.DMA((n,)), pltpu.SemaphoreType.DMA((n,))], start, wait)


def _chip_exchange_rider(to_send, by_chip, row_range=None):
    n = len(to_send)

    def copies(r_in, r_out, sems):
        x, y, c = _mesh_pos()
        me = 2 * x + y
        rows = (lambda ref: ref) if row_range is None else (lambda ref: ref.at[pl.ds(*row_range)])
        out = []
        for a in range(n):
            for k, (px, py) in enumerate(_other_chips(x, y)):
                send = functools.partial(
                    pltpu.make_async_remote_copy,
                    src_ref=rows(r_in[a].at[2 * px + py]), dst_ref=rows(r_out[a].at[me]),
                    send_sem=sems[0].at[a, k], recv_sem=sems[1].at[a, k], device_id=(px, py, c),
                    device_id_type=MESH)
                recv = functools.partial(
                    pltpu.make_async_remote_copy,
                    src_ref=rows(r_in[a].at[me]), dst_ref=rows(r_out[a].at[2 * px + py]),
                    send_sem=sems[0].at[a, k], recv_sem=sems[1].at[a, k], device_id=(px, py, c),
                    device_id_type=MESH)
                out.append((send, recv))
        return out

    def start(r_in, r_out, sems):
        for send, _ in copies(r_in, r_out, sems):
            send().start()

    def wait(r_in, r_out, sems):
        cps = copies(r_in, r_out, sems)
        for _, recv in cps:
            recv().wait_recv()
        for send, _ in cps:
            send().wait_send()

    return _Rider(list(to_send) + list(by_chip), [jax.ShapeDtypeStruct(b.shape, b.dtype) for b in by_chip],
                  {n + i: i for i in range(n)},
                  [pltpu.SemaphoreType.DMA((n, 3)), pltpu.SemaphoreType.DMA((n, 3))], start, wait)


HBM = pl.BlockSpec(memory_space=pltpu.HBM)
SEM = pl.BlockSpec(memory_space=pltpu.SEMAPHORE)


_IN_FLIGHT = pltpu.CompilerParams(has_side_effects=pltpu.SideEffectType.DATAFLOW_SIDE_EFFECTING)


class _FlatSems:
    def __init__(self, ref, shape):
        self.ref, self.shape = ref, shape

    @property
    def at(self):
        return self

    def __getitem__(self, idx):
        idx = idx if isinstance(idx, tuple) else (idx,)
        flat = 0
        for i, n in zip(idx, self.shape):
            flat = flat * n + i
        return self.ref.at[flat]


def _flat_sem_types(rider):
    return tuple(pltpu.SemaphoreType.DMA((int(np.prod(s.shape)),)) for s in rider.scratch)


def _as_rider_sems(rider, refs):
    return [_FlatSems(r, s.shape) for r, s in zip(refs, rider.scratch)]


def _split_start(rider, name, after=()):
    n_in, n_out, n_sem = len(rider.operands), len(rider.out_shapes), len(rider.scratch)
    n_after = len(after)
    fresh = [j for j in range(n_out) if j not in rider.aliases.values()]
    by_out = {j: i for i, j in rider.aliases.items()}

    def body(*refs):
        r_in = refs[:n_in]
        refs = refs[n_in + n_after:]
        sems = refs[:n_sem]
        thru = refs[n_sem:n_sem + n_in]
        fresh_refs = refs[n_sem + n_in:n_sem + n_in + len(fresh)]
        token = refs[-1]
        r_out = [thru[by_out[j]] if j in by_out else fresh_refs[fresh.index(j)] for j in range(n_out)]
        rider.start(r_in, r_out, _as_rider_sems(rider, sems))
        token[...] = jnp.zeros_like(token)

    res = pl.pallas_call(
        body, name=name,
        out_shape=_flat_sem_types(rider) + tuple(pltpu.HBM(o.shape, o.dtype) for o in rider.operands)
        + tuple(pltpu.HBM(rider.out_shapes[j].shape, rider.out_shapes[j].dtype) for j in fresh)
        + (jax.ShapeDtypeStruct((8, LANES), F32),),
        in_specs=(HBM,) * n_in + (ANY,) * n_after,
        out_specs=(SEM,) * n_sem + (HBM,) * (n_in + len(fresh)) + (pl.BlockSpec(memory_space=pltpu.VMEM),),
        input_output_aliases={i: n_sem + i for i in range(n_in)}, compiler_params=_IN_FLIGHT,
    )(*[pltpu.with_memory_space_constraint(o, pltpu.HBM) for o in rider.operands], *after)
    return (rider, res[:n_sem], res[n_sem:n_sem + n_in], res[n_sem + n_in:-1]), res[-1]


def _split_continue(handles, after, name, phase):
    rider, sems, thru, fresh_arrays = handles
    n_in, n_out, n_sem = len(rider.operands), len(rider.out_shapes), len(rider.scratch)
    fresh = [j for j in range(n_out) if j not in rider.aliases.values()]
    by_out = {j: i for i, j in rider.aliases.items()}
    n_data = n_in + len(fresh)

    def body(*refs):
        r_in = refs[:n_in]
        fresh_refs = refs[n_in:n_data]
        sem_refs = refs[n_data:n_data + n_sem]
        r_out = [r_in[by_out[j]] if j in by_out else fresh_refs[fresh.index(j)] for j in range(n_out)]
        phase(r_in, r_out, _as_rider_sems(rider, sem_refs))

    data = list(thru) + list(fresh_arrays)
    return pl.pallas_call(
        body, name=name, out_shape=tuple(pltpu.HBM(d.shape, d.dtype) for d in data),
        in_specs=(HBM,) * n_data + (SEM,) * n_sem + (ANY,) * len(after), out_specs=(HBM,) * n_data,
        input_output_aliases={i: i for i in range(n_data)}, compiler_params=_IN_FLIGHT,
    )(*data, *sems, *after)


def _split_middle(handles, after, name):
    rider, sems, thru, _ = handles
    res = _split_continue(handles, after, name, rider.middle)
    return rider, sems, res[:len(thru)], res[len(thru):]


def _split_wait(handles, after, name):
    rider = handles[0]
    n_in, n_out = len(rider.operands), len(rider.out_shapes)
    fresh = [j for j in range(n_out) if j not in rider.aliases.values()]
    by_out = {j: i for i, j in rider.aliases.items()}
    res = _split_continue(handles, after, name, rider.wait)
    return [res[by_out[j]] if j in by_out else res[n_in + fresh.index(j)] for j in range(n_out)], res[:n_in]


def _pair_gather_rider(bufs):
    n = len(bufs)

    def copies(r_out, sems):
        x, y, c = _mesh_pos()
        out = []
        for a in range(n):
            send = functools.partial(
                    pltpu.make_async_remote_copy,
                src_ref=r_out[a].at[c], dst_ref=r_out[a].at[c], send_sem=sems[0].at[a],
                recv_sem=sems[1].at[a], device_id=(x, y, 1 - c), device_id_type=MESH)
            recv = functools.partial(
                    pltpu.make_async_remote_copy,
                src_ref=r_out[a].at[1 - c], dst_ref=r_out[a].at[1 - c], send_sem=sems[0].at[a],
                recv_sem=sems[1].at[a], device_id=(x, y, 1 - c), device_id_type=MESH)
            out.append((send, recv))
        return out

    def start(r_in, r_out, sems):
        for send, _ in copies(r_out, sems):
            send().start()

    def wait(r_in, r_out, sems):
        cps = copies(r_out, sems)
        for _, recv in cps:
            recv().wait_recv()
        for send, _ in cps:
            send().wait_send()

    return _Rider(bufs, [jax.ShapeDtypeStruct(b.shape, b.dtype) for b in bufs], {i: i for i in range(n)},
                  [pltpu.SemaphoreType.DMA((n,)), pltpu.SemaphoreType.DMA((n,))], start, wait)


def _add_own_half(g, recv, pos_arr, name):
    nb, rh, cols = g.shape[0], g.shape[-2], g.shape[-1]

    def body(pos_ref, g_ref, r_ref, send_ref, own_ref):
        s = (g_ref[...] + r_ref[...]).astype(BF16)
        send_ref[...] = s

        @pl.when(pl.program_id(0) == pos_ref[1])
        def _():
            own_ref[...] = s

    blk = pl.BlockSpec((None, rh, cols), lambda j, pos_ref: (j, 0, 0))
    g_spec = blk if g.ndim == 3 else pl.BlockSpec((None, None, rh, cols),
                                                   lambda j, pos_ref: (j, pos_ref[0], 0, 0))
    shape = jax.ShapeDtypeStruct((nb, rh, cols), BF16)
    return pl.pallas_call(
        body, name=name,
        grid_spec=pltpu.PrefetchScalarGridSpec(
            num_scalar_prefetch=1, grid=(nb,), in_specs=[g_spec, blk],
            out_specs=[blk, pl.BlockSpec((None, rh, cols), lambda j, pos_ref: (pos_ref[1], 0, 0))]),
        out_shape=[shape, shape], compiler_params=_params(),
    )(pos_arr, g, recv)


def _sum_chips(gath, pos_arr, name):
    nb, rh, cols = gath.shape

    def body(pos_ref, a_ref, b_ref, c_ref, d_ref, o_ref):
        del pos_ref
        o_ref[...] = ((a_ref[...].astype(F32) + b_ref[...].astype(F32)) + c_ref[...].astype(F32)) \
            + d_ref[...].astype(F32)

    tr = rh // 2 if (rh // 2) % 16 == 0 else rh
    specs = [pl.BlockSpec((None, tr, cols), functools.partial(lambda i, pos_ref, j: (j, i, 0), j=j))
             for j in range(nb)]
    return pl.pallas_call(
        body, name=name,
        grid_spec=pltpu.PrefetchScalarGridSpec(
            num_scalar_prefetch=1, grid=(rh // tr,), in_specs=specs,
            out_specs=pl.BlockSpec((None, tr, cols), lambda i, pos_ref: (pos_ref[0], i, 0))),
        out_shape=jax.ShapeDtypeStruct((2, rh, cols), F32), compiler_params=_params(),
    )(pos_arr, gath, gath, gath, gath)


N_DEVICES = 8


def _small_gather_rider(buf):
    def copies(r_out, sems):
        x, y, c = _mesh_pos()
        me = 4 * x + 2 * y + c
        out = []
        for r in range(1, N_DEVICES):
            px = 1 - x if r & 4 else x
            py = 1 - y if r & 2 else y
            pc = 1 - c if r & 1 else c
            out.append(pltpu.make_async_remote_copy(
                src_ref=r_out[0].at[me], dst_ref=r_out[0].at[me], send_sem=sems[0].at[r - 1],
                recv_sem=sems[1].at[r - 1], device_id=(px, py, pc), device_id_type=MESH))
        return out

    def start(r_in, r_out, sems):
        for cp in copies(r_out, sems):
            cp.start()

    def wait(r_in, r_out, sems):
        cps = copies(r_out, sems)
        for cp in cps:
            cp.wait_recv()
        for cp in cps:
            cp.wait_send()

    return _Rider([buf], [jax.ShapeDtypeStruct(buf.shape, buf.dtype)], {0: 0},
                  [pltpu.SemaphoreType.DMA((N_DEVICES - 1,)), pltpu.SemaphoreType.DMA((N_DEVICES - 1,))],
                  start, wait)


def _sum_devices(buf, name):
    def body(b_ref, o_ref):
        acc = b_ref[0]
        for i in range(1, N_DEVICES):
            acc = acc + b_ref[i]
        o_ref[...] = acc

    return _pallas(body, name=name, grid=(1,), in_specs=[pl.BlockSpec(buf.shape, lambda i: (0, 0, 0))],
                   out_specs=pl.BlockSpec(buf.shape[1:], lambda i: (0, 0)),
                   out_shape=jax.ShapeDtypeStruct(buf.shape[1:], F32), operands=[buf])


def _adamw_math(w, g, m, v):
    m = ADAM_B1 * m + (1.0 - ADAM_B1) * g
    v = ADAM_B2 * v + (1.0 - ADAM_B2) * (g * g)
    m_hat = m / (1.0 - ADAM_B1 ** ADAM_STEP)
    v_hat = v / (1.0 - ADAM_B2 ** ADAM_STEP)
    delta = -ADAM_LR * (m_hat / (jnp.sqrt(v_hat) + ADAM_EPS) + ADAM_WD * w)
    return delta, m, v


def _adamw(w, g, m, v, name, after=()):
    r, c = w.shape
    tr = 128 if r % 128 == 0 else 64
    assert r % tr == 0

    def body(w_ref, g_ref, m_ref, v_ref, go_ref, d_ref, mo_ref, vo_ref):
        gv = g_ref[...]
        d, mn, vn = _adamw_math(w_ref[...], gv, m_ref[...], v_ref[...])
        go_ref[...] = gv
        d_ref[...] = d
        mo_ref[...] = mn
        vo_ref[...] = vn

    blk = pl.BlockSpec((tr, c), lambda i: (i, 0))
    return _pallas(body, name=name, grid=(r // tr,), in_specs=[blk] * 4, out_specs=[blk] * 4,
                   out_shape=[jax.ShapeDtypeStruct((r, c), F32)] * 4, operands=[w, g, m, v], after=after)


def _adamw_small(ws, gs, ms, vs, name):
    n = len(ws)

    def body(*refs):
        w_r, g_r, m_r, v_r = refs[:n], refs[n:2 * n], refs[2 * n:3 * n], refs[3 * n:4 * n]
        d_o, m_o, v_o = refs[4 * n:5 * n], refs[5 * n:6 * n], refs[6 * n:7 * n]
        for i in range(n):
            d, mn, vn = _adamw_math(w_r[i][...], g_r[i][...], m_r[i][...], v_r[i][...])
            d_o[i][...] = d
            m_o[i][...] = mn
            v_o[i][...] = vn

    specs = [pl.BlockSpec(w.shape, lambda i: (0, 0)) for w in ws]
    shapes = [jax.ShapeDtypeStruct(w.shape, F32) for w in ws]
    outs = pl.pallas_call(
        body, name=name, grid=(1,), in_specs=specs * 4, out_specs=specs * 3, out_shape=shapes * 3,
        compiler_params=_params(),
    )(*ws, *gs, *ms, *vs)
    return outs[:n], outs[n:2 * n], outs[2 * n:]


BIG = ("w_in", "w_o_attn", "w_pw_conv", "w_out", "w_ffn_in", "w_ffn_out")
ROW_SHARDED = ("w_out", "w_ffn_out")
SMALL = ("norm1_w", "b_gate", "q_norm_w", "k_norm_w", "conv_w", "conv_b", "conv_ln_w", "conv_ln_b", "norm2_w")
ORDER = ("norm1_w", "w_in", "b_gate", "q_norm_w", "k_norm_w", "w_o_attn", "conv_w", "conv_b", "conv_ln_w",
         "conv_ln_b", "w_pw_conv", "w_out", "norm2_w", "w_ffn_in", "w_ffn_out")
PACK_TILE = 8 * LANES


def _pack_small(parts):
    rows = []
    for p in parts:
        flat = p.reshape(-1)
        pad = (-flat.shape[0]) % PACK_TILE
        rows.append(jnp.pad(flat, (0, pad)).reshape(-1, LANES))
    return jnp.concatenate(rows, axis=0)


def _unpack_small(packed, shapes):
    out, row = [], 0
    for shp in shapes:
        size = int(np.prod(shp))
        nrow = -(-size // PACK_TILE) * (PACK_TILE // LANES)
        out.append(packed[row:row + nrow].reshape(-1)[:size].reshape(shp))
        row += nrow
    return out


def kernel(x, positions, norm1_w, w_in, b_gate, q_norm_w, k_norm_w, w_o_attn, conv_w, conv_b, conv_ln_w, conv_ln_b, w_pw_conv, w_out, norm2_w, w_ffn_in, w_ffn_out, loss_target, m_norm1_w, m_w_in, m_b_gate, m_q_norm_w, m_k_norm_w, m_w_o_attn, m_conv_w, m_conv_b, m_conv_ln_w, m_conv_ln_b, m_w_pw_conv, m_w_out, m_norm2_w, m_w_ffn_in, m_w_ffn_out, v_norm1_w, v_w_in, v_b_gate, v_q_norm_w, v_k_norm_w, v_w_o_attn, v_conv_w, v_conv_b, v_conv_ln_w, v_conv_ln_b, v_w_pw_conv, v_w_out, v_norm2_w, v_w_ffn_in, v_w_ffn_out):
    w = dict(norm1_w=norm1_w, w_in=w_in, b_gate=b_gate, q_norm_w=q_norm_w, k_norm_w=k_norm_w, w_o_attn=w_o_attn,
             conv_w=conv_w, conv_b=conv_b, conv_ln_w=conv_ln_w, conv_ln_b=conv_ln_b, w_pw_conv=w_pw_conv,
             w_out=w_out, norm2_w=norm2_w, w_ffn_in=w_ffn_in, w_ffn_out=w_ffn_out)
    m = dict(norm1_w=m_norm1_w, w_in=m_w_in, b_gate=m_b_gate, q_norm_w=m_q_norm_w, k_norm_w=m_k_norm_w,
             w_o_attn=m_w_o_attn, conv_w=m_conv_w, conv_b=m_conv_b, conv_ln_w=m_conv_ln_w,
             conv_ln_b=m_conv_ln_b, w_pw_conv=m_w_pw_conv, w_out=m_w_out, norm2_w=m_norm2_w,
             w_ffn_in=m_w_ffn_in, w_ffn_out=m_w_ffn_out)
    v = dict(norm1_w=v_norm1_w, w_in=v_w_in, b_gate=v_b_gate, q_norm_w=v_q_norm_w, k_norm_w=v_k_norm_w,
             w_o_attn=v_w_o_attn, conv_w=v_conv_w, conv_b=v_conv_b, conv_ln_w=v_conv_ln_w,
             conv_ln_b=v_conv_ln_b, w_pw_conv=v_w_pw_conv, w_out=v_w_out, norm2_w=v_norm2_w,
             w_ffn_in=v_w_ffn_in, w_ffn_out=v_w_ffn_out)
    cx, cy, cc = _mesh_pos()
    chip = 2 * cx + cy

    chip_arr = chip.reshape(1).astype(jnp.int32)
    pos_arr = jnp.stack([cc, chip]).astype(jnp.int32)
    halves = lambda buf: buf.reshape(N_CHIPS, 2, buf.shape[1] // 2, buf.shape[2])
    w_in_buf = halves(_cast_into_slot(w["w_in"][0], chip_arr, BF16, "cast_w_in"))
    small_bufs = [_cast_into_slot(w[n][0], chip_arr, F32, f"slot_{n}") for n in ("conv_w", "b_gate")]
    first_gather, started = _split_start(_gather_both_legs_rider([w_in_buf], small_bufs, NEAR_CHIPS),
                                         "allgather_w_in_near_start")
    late_bufs = [halves(_cast_into_slot(w[n][0], chip_arr, BF16, f"cast_{n}", after=[started]))
                 for n in LATE_GATHER]
    wts = dict(norm1_w=norm1_w, q_norm_w=q_norm_w, k_norm_w=k_norm_w, conv_b=conv_b, conv_ln_w=conv_ln_w,
               conv_ln_b=conv_ln_b, norm2_w=norm2_w)

    loss, grad_x, g, reduced, w_in_in_flight = _forward_backward(
        x[0], positions.reshape(-1, 1), loss_target[0], wts, first_gather, late_bufs, pos_arr)
    grads = {n: b.reshape(-1, b.shape[2]) for n, b in reduced.items()}

    w_in_in_flight, started = w_in_in_flight
    delta, new_m, new_v = {}, {}, {}
    for n in EARLY_REDUCE:
        grads[n], delta[n], new_m[n], new_v[n] = _adamw(w[n][0], grads[n], m[n][0], v[n][0], f"adamw_{n}",
                                                        after=[started])
    small_parts = [loss] + [g[n] for n in SMALL]
    small_shapes = [p.shape for p in small_parts]
    device_arr = (4 * cx + 2 * cy + cc).reshape(1).astype(jnp.int32)
    small_buf = _cast_into_slot(_pack_small(small_parts), device_arr, F32, "slot_small", n_slots=N_DEVICES)

    (by_chip_w_in,), _ = _split_wait(w_in_in_flight, after=[delta[n] for n in EARLY_REDUCE] + [small_buf],
                                     name="grads_w_in_exchange_wait")
    half_w_in = _sum_chips(by_chip_w_in, pos_arr, "grads_chip_sum_w_in")
    shard_w_in, small_buf = _comm_call(
        _riders_together(_pair_gather_rider([half_w_in]), _small_gather_rider(small_buf)),
        "grads_pair_gather_w_in_small_gather")
    summed = _sum_devices(small_buf, "small_sum")
    reduced = _unpack_small(summed, small_shapes)
    loss_total = reduced[0].reshape(())
    for n, r in zip(SMALL, reduced[1:]):
        grads[n] = r
    ch_shard = conv_w.shape[2]
    grads["conv_w"] = lax.dynamic_slice_in_dim(grads["conv_w"], chip * ch_shard, ch_shard, axis=1)
    d_shard = b_gate.shape[2]
    grads["b_gate"] = lax.dynamic_slice_in_dim(grads["b_gate"], chip * d_shard, d_shard, axis=1)

    grads["w_in"], delta["w_in"], new_m["w_in"], new_v["w_in"] = _adamw(
        w["w_in"][0], shard_w_in.reshape(-1, shard_w_in.shape[2]), m["w_in"][0], v["w_in"][0], "adamw_w_in")
    flat2 = lambda a: a.reshape(-1, a.shape[-1])
    d_s, m_s, v_s = _adamw_small([flat2(w[n]) for n in SMALL], [flat2(grads[n]) for n in SMALL],
                                 [flat2(m[n]) for n in SMALL], [flat2(v[n]) for n in SMALL], "adamw_small")
    for i, n in enumerate(SMALL):
        delta[n], new_m[n], new_v[n] = d_s[i], m_s[i], v_s[i]

    shaped = lambda d, n: d[n].reshape(w[n].shape)
    return (loss_total, grad_x[None], *[shaped(grads, n) for n in ORDER], *[shaped(delta, n) for n in ORDER],
            *[shaped(new_m, n) for n in ORDER], *[shaped(new_v, n) for n in ORDER])
```

```python
import functools

import numpy as np
import jax
import jax.numpy as jnp
from jax import lax
from jax.experimental import pallas as pl
from jax.experimental.pallas import tpu as pltpu

F32 = jnp.float32
BF16 = jnp.bfloat16
MESH = pl.DeviceIdType.MESH
ANY = pl.BlockSpec(memory_space=pl.ANY)

HEAD_DIM = 64
N_SLOT_HEADS = 8
DILATIONS = (1, 4, 16)
HALF_SPAN = 64
ROPE_THETA = 500000.0
ROT_DIM = 16
CONV_WIDTH = 31
EPS = 1e-6
NEG_INF = -1e30
ADAM_LR, ADAM_B1, ADAM_B2, ADAM_EPS, ADAM_WD, ADAM_STEP = 0.001, 0.9, 0.999, 1e-08, 0.01, 10

LANES = 128
QBLK = 128
KWIN = QBLK + 2 * HALF_SPAN
VMEM_LIMIT = 48 * 1024 * 1024
N_CHIPS = 4


def _params(**kw):
    return pltpu.CompilerParams(vmem_limit_bytes=VMEM_LIMIT, **kw)


class _Rider:
    def __init__(self, operands, out_shapes, aliases, scratch, start, wait, middle=None):
        self.operands, self.out_shapes, self.aliases = list(operands), list(out_shapes), dict(aliases)
        self.scratch, self.start, self.wait = list(scratch), start, wait
        self.middle = middle


def _riders_together(a, b):
    n_in, n_out, n_sc = len(a.operands), len(a.out_shapes), len(a.scratch)
    aliases = dict(a.aliases)
    aliases.update({n_in + src: n_out + dst for src, dst in b.aliases.items()})

    def start(r_in, r_out, r_sc):
        a.start(r_in[:n_in], r_out[:n_out], r_sc[:n_sc])
        b.start(r_in[n_in:], r_out[n_out:], r_sc[n_sc:])

    def wait(r_in, r_out, r_sc):
        a.wait(r_in[:n_in], r_out[:n_out], r_sc[:n_sc])
        b.wait(r_in[n_in:], r_out[n_out:], r_sc[n_sc:])

    return _Rider(a.operands + b.operands, a.out_shapes + b.out_shapes, aliases, a.scratch + b.scratch, start, wait)


def _pallas(body, *, name, grid, in_specs, out_specs, out_shape, operands, scratch_shapes=(), aliases=None,
            rider=None, after=()):
    single = not isinstance(out_specs, (list, tuple))
    out_specs_l = [out_specs] if single else list(out_specs)
    out_shape_l = [out_shape] if single else list(out_shape)
    aliases = dict(aliases or {})

    def call(fn, all_in_specs, all_out_specs, all_out_shape, all_scratch, all_aliases, all_operands):
        return pl.pallas_call(
            fn, name=name, grid=grid, in_specs=all_in_specs, out_specs=all_out_specs, out_shape=all_out_shape,
            scratch_shapes=all_scratch, input_output_aliases=all_aliases, compiler_params=_params(),
        )(*all_operands)

    if rider is None:
        n_main = len(in_specs)

        def ordered(*refs):
            body(*refs[:n_main], *refs[n_main + len(after):])

        res = call(ordered if after else body, list(in_specs) + [ANY] * len(after), out_specs_l, out_shape_l,
                   list(scratch_shapes), aliases, list(operands) + list(after))
        return res[0] if single else res
    assert not after
    n_in, n_rin = len(in_specs), len(rider.operands)
    n_out, n_rout = len(out_specs_l), len(rider.out_shapes)
    n_sc = len(scratch_shapes)

    def wrapped(*refs):
        main_in, r_in = refs[:n_in], refs[n_in:n_in + n_rin]
        o0 = n_in + n_rin
        main_out, r_out = refs[o0:o0 + n_out], refs[o0 + n_out:o0 + n_out + n_rout]
        s0 = o0 + n_out + n_rout
        main_sc, r_sc = refs[s0:s0 + n_sc], refs[s0 + n_sc:]
        ids = [pl.program_id(d) for d in range(len(grid))]
        first = functools.reduce(jnp.logical_and, [i == 0 for i in ids])
        last = functools.reduce(jnp.logical_and, [i == n - 1 for i, n in zip(ids, grid)])

        @pl.when(first)
        def _():
            rider.start(r_in, r_out, r_sc)

        body(*main_in, *main_out, *main_sc)

        @pl.when(last)
        def _():
            rider.wait(r_in, r_out, r_sc)

    for src, dst in rider.aliases.items():
        aliases[n_in + src] = n_out + dst
    res = call(wrapped, list(in_specs) + [ANY] * n_rin, out_specs_l + [ANY] * n_rout,
               out_shape_l + rider.out_shapes, list(scratch_shapes) + rider.scratch, aliases,
               list(operands) + rider.operands)
    main = res[:n_out]
    return (main[0] if single else main), res[n_out:]


def _matmul(a, b, *, mode, tm, tn, tk, out_dtype, name, b_blocked=False,
            out_blocked=None, cols_outer=False, rider=None, after=()):
    a_shape = a.shape
    if mode == "nn":
        m_dim, k_dim = a_shape
        n_dim = b.shape[0] * b.shape[2] if b_blocked else b.shape[1]
        rows, cols, red = m_dim, n_dim, k_dim
    elif mode == "nt":
        m_dim, n_dim = a_shape
        k_dim = b.shape[1] if b_blocked else b.shape[0]
        rows, cols, red = m_dim, k_dim, n_dim
    else:
        m_dim, k_dim = a_shape
        n_dim = b.shape[1]
        rows, cols, red = k_dim, n_dim, m_dim
    assert rows % tm == 0 and cols % tn == 0 and red % tk == 0, (name, rows, cols, red)
    ni, nj, nk = rows // tm, cols // tn, red // tk

    if mode == "nn":
        a_spec = pl.BlockSpec((tm, tk), lambda i, j, k: (i, k))
        if b_blocked:
            per = b.shape[2] // tn
            b_spec = pl.BlockSpec((None, tk, tn), lambda i, j, k: (j // per, k, j % per))
        else:
            b_spec = pl.BlockSpec((tk, tn), lambda i, j, k: (k, j))
        dims = (((1,), (0,)), ((), ()))
    elif mode == "nt":
        a_spec = pl.BlockSpec((tm, tk), lambda i, j, k: (i, k))
        if b_blocked:
            per = b.shape[2] // tk
            b_spec = pl.BlockSpec((None, tn, tk), lambda i, j, k: (k // per, j, k % per))
        else:
            b_spec = pl.BlockSpec((tn, tk), lambda i, j, k: (j, k))
        dims = (((1,), (1,)), ((), ()))
    else:
        a_spec = pl.BlockSpec((tk, tm), lambda i, j, k: (k, i))
        b_spec = pl.BlockSpec((tk, tn), lambda i, j, k: (k, j))
        dims = (((0,), (0,)), ((), ()))

    if out_blocked:
        per_o = (cols // out_blocked) // tn
        out_spec = pl.BlockSpec((None, tm, tn), lambda i, j, k: (j // per_o, i, j % per_o))
        out_shape = jax.ShapeDtypeStruct((out_blocked, rows, cols // out_blocked), out_dtype)
    else:
        out_spec = pl.BlockSpec((tm, tn), lambda i, j, k: (i, j))
        out_shape = jax.ShapeDtypeStruct((rows, cols), out_dtype)

    def body(a_ref, b_ref, o_ref, *acc):
        prod = lax.dot_general(a_ref[...], b_ref[...], dims, preferred_element_type=F32)
        if nk == 1:
            o_ref[...] = prod.astype(out_dtype)
        else:
            acc_ref, = acc
            k = pl.program_id(2)

            @pl.when(k == 0)
            def _():
                acc_ref[...] = prod

            @pl.when(k > 0)
            def _():
                acc_ref[...] += prod

            @pl.when(k == nk - 1)
            def _():
                o_ref[...] = acc_ref[...].astype(out_dtype)

    scratch = [pltpu.VMEM((tm, tn), F32)] if nk > 1 else []
    grid = (ni, nj, nk)
    if cols_outer:
        swap = lambda spec: pl.BlockSpec(spec.block_shape, lambda j, i, k, f=spec.index_map: f(i, j, k))
        a_spec, b_spec, out_spec, grid = swap(a_spec), swap(b_spec), swap(out_spec), (nj, ni, nk)
    return _pallas(body, name=name, grid=grid, in_specs=[a_spec, b_spec], out_specs=out_spec,
                   out_shape=out_shape, operands=[a, b], scratch_shapes=scratch, rider=rider, after=after)


def _proj_by_slot(h, w_in, order, first, count, name, proj=None, after=()):
    s, k = h.shape
    nb = w_in.shape[2]
    tm = 512
    prev = [] if proj is None else [proj]

    def body(order_ref, h_ref, w_ref, *refs):
        refs[-1][...] = jnp.dot(h_ref[...], w_ref[...], preferred_element_type=F32)

    return pl.pallas_call(
        body, name=name,
        grid_spec=pltpu.PrefetchScalarGridSpec(
            num_scalar_prefetch=1, grid=(count, s // tm),
            in_specs=[pl.BlockSpec((tm, k), lambda j, i, order_ref: (i, 0)),
                      pl.BlockSpec((None, k, nb), lambda j, i, order_ref: (order_ref[first + j], 0, 0))]
            + [ANY] * (len(prev) + len(after)),
            out_specs=pl.BlockSpec((tm, nb), lambda j, i, order_ref: (i, order_ref[first + j]))),
        out_shape=jax.ShapeDtypeStruct((s, N_CHIPS * nb), F32),
        input_output_aliases={3: 0} if prev else {}, compiler_params=_params(),
    )(order, h, w_in, *prev, *after)


def _rmsnorm_fwd(x, w, name):
    s, d = x.shape
    tm = 256

    def body(x_ref, w_ref, o_ref):
        xv = x_ref[...]
        rstd = lax.rsqrt(jnp.mean(xv * xv, axis=-1, keepdims=True) + EPS)
        o_ref[...] = (xv * rstd * w_ref[...]).astype(BF16)

    return pl.pallas_call(
        body, name=name, grid=(s // tm,),
        in_specs=[pl.BlockSpec((tm, d), lambda i: (i, 0)), pl.BlockSpec((1, d), lambda i: (0, 0))],
        out_specs=pl.BlockSpec((tm, d), lambda i: (i, 0)),
        out_shape=jax.ShapeDtypeStruct((s, d), BF16), compiler_params=_params(),
    )(x, w)


def _rmsnorm_bwd(dh, x, w, dres, name, rider=None):
    s, d = x.shape
    tm = 256

    def body(dh_ref, x_ref, w_ref, dres_ref, dx_ref, dxb_ref, dw_ref):
        xv = x_ref[...]
        rstd = lax.rsqrt(jnp.mean(xv * xv, axis=-1, keepdims=True) + EPS)
        xhat = xv * rstd
        dhv = dh_ref[...]
        g = dhv * w_ref[...]
        dx = rstd * (g - xhat * jnp.mean(g * xhat, axis=-1, keepdims=True)) + dres_ref[...]
        dx_ref[...] = dx
        dxb_ref[...] = dx.astype(BF16)
        part = jnp.sum(dhv * xhat, axis=0, keepdims=True)

        @pl.when(pl.program_id(0) == 0)
        def _():
            dw_ref[...] = part

        @pl.when(pl.program_id(0) > 0)
        def _():
            dw_ref[...] += part

    row = pl.BlockSpec((tm, d), lambda i: (i, 0))
    vec = pl.BlockSpec((1, d), lambda i: (0, 0))
    return _pallas(
        body, name=name, grid=(s // tm,), in_specs=[row, row, vec, row], out_specs=[row, row, vec],
        out_shape=[jax.ShapeDtypeStruct((s, d), F32), jax.ShapeDtypeStruct((s, d), BF16),
                   jax.ShapeDtypeStruct((1, d), F32)],
        operands=[dh, x, w, dres], rider=rider)


def _rope_consts():
    lane = np.arange(LANES)
    in_head = lane % HEAD_DIM
    inv_freq = ROPE_THETA ** (-jnp.arange(0, ROT_DIM, 2, dtype=F32) / ROT_DIM)
    invf = jnp.where(jnp.asarray(in_head < ROT_DIM), jnp.tile(inv_freq, LANES // (ROT_DIM // 2)), 0.0)
    m_a = np.where(in_head < ROT_DIM // 2, -1.0, 0.0).astype(np.float32)
    m_b = np.where((in_head >= ROT_DIM // 2) & (in_head < ROT_DIM), 1.0, 0.0).astype(np.float32)
    block_diag = (lane[:, None] // HEAD_DIM == lane[None, :] // HEAD_DIM).astype(np.float32)
    return (invf.reshape(1, LANES).astype(F32), jnp.asarray(m_a).reshape(1, LANES),
            jnp.asarray(m_b).reshape(1, LANES), jnp.asarray(block_diag, dtype=BF16))


def _head_sums(v, bd):
    hi = v.astype(BF16)
    lo = (v - hi.astype(F32)).astype(BF16)
    return jnp.dot(hi, bd, preferred_element_type=F32) + jnp.dot(lo, bd, preferred_element_type=F32)


def _qk_fwd(proj, pos_col, qw2, kw2, consts, name, rider=None):
    s = proj.shape[0]
    width = 3 * N_SLOT_HEADS * HEAD_DIM
    tm = 128
    invf, m_a, m_b, bd = consts
    scale = HEAD_DIM ** -0.5

    def body(q_ref, k_ref, pos_ref, qw_ref, kw_ref, invf_ref, ma_ref, mb_ref, bd_ref, qo_ref, ko_ref):
        ang = pos_ref[...].astype(F32) * invf_ref[...]
        cos = jnp.cos(ang)
        sin = jnp.sin(ang)
        s_a = sin * ma_ref[...]
        s_b = sin * mb_ref[...]
        bdv = bd_ref[...]
        for src, w_ref, dst, sc in ((q_ref, qw_ref, qo_ref, scale), (k_ref, kw_ref, ko_ref, 1.0)):
            for cb in range(width // LANES):
                cols = slice(cb * LANES, (cb + 1) * LANES)
                t = src[:, cols]
                rstd = lax.rsqrt(_head_sums(t * t, bdv) * (1.0 / HEAD_DIM) + EPS)
                y = t * rstd * w_ref[...]
                r = y * cos + pltpu.roll(y, LANES - 8, axis=1) * s_a + pltpu.roll(y, 8, axis=1) * s_b
                dst[:, cols] = r * sc if sc != 1.0 else r

    vec = pl.BlockSpec((1, LANES), lambda i: (0, 0))
    return _pallas(
        body, name=name, grid=(s // tm,),
        in_specs=[pl.BlockSpec((tm, width), lambda i: (i, 0)), pl.BlockSpec((tm, width), lambda i: (i, 1)),
                  pl.BlockSpec((tm, 1), lambda i: (i, 0)), vec, vec, vec, vec, vec,
                  pl.BlockSpec((LANES, LANES), lambda i: (0, 0))],
        out_specs=[pl.BlockSpec((tm, width), lambda i: (i, 0))] * 2,
        out_shape=[jax.ShapeDtypeStruct((s, width), F32)] * 2,
        operands=[proj, proj, pos_col, qw2, kw2, invf, m_a, m_b, bd], rider=rider)


def _qk_bwd(dqn, dkn, dv, da, db, dgl, proj, pos_col, qw2, kw2, consts, name, rider=None):
    s = proj.shape[0]
    width = 3 * N_SLOT_HEADS * HEAD_DIM
    ch = da.shape[1]
    gate_w = dgl.shape[1]
    out_w = 3 * width + 2 * ch + gate_w
    assert out_w == proj.shape[1]
    tm = 128
    invf, m_a, m_b, bd = consts
    scale = HEAD_DIM ** -0.5

    def body(dq_ref, dk_ref, dv_ref, da_ref, db_ref, dgl_ref, q_ref, k_ref, pos_ref, qw_ref, kw_ref,
             invf_ref, ma_ref, mb_ref, bd_ref, out_ref, dqw_ref, dkw_ref):
        ang = pos_ref[...].astype(F32) * invf_ref[...]
        cos = jnp.cos(ang)
        sin = jnp.sin(ang)
        s_a = sin * ma_ref[...]
        s_b = sin * mb_ref[...]
        bdv = bd_ref[...]
        first = pl.program_id(0) == 0
        for src, dsrc, w_ref, col0, dw_ref, sc in ((q_ref, dq_ref, qw_ref, 0, dqw_ref, scale),
                                                   (k_ref, dk_ref, kw_ref, width, dkw_ref, 1.0)):
            dw_acc = jnp.zeros((1, LANES), F32)
            for cb in range(width // LANES):
                cols = slice(cb * LANES, (cb + 1) * LANES)
                t = src[:, cols]
                dr = dsrc[:, cols]
                if sc != 1.0:
                    dr = dr * sc
                dy = dr * cos + pltpu.roll(dr * s_a, 8, axis=1) + pltpu.roll(dr * s_b, LANES - 8, axis=1)
                rstd = lax.rsqrt(_head_sums(t * t, bdv) * (1.0 / HEAD_DIM) + EPS)
                xhat = t * rstd
                g = dy * w_ref[...]
                dt = rstd * (g - xhat * (_head_sums(g * xhat, bdv) * (1.0 / HEAD_DIM)))
                out_ref[:, col0 + cb * LANES: col0 + (cb + 1) * LANES] = dt.astype(BF16)
                dw_acc = dw_acc + jnp.sum(dy * xhat, axis=0, keepdims=True)
            dw_acc = dw_acc + pltpu.roll(dw_acc, HEAD_DIM, axis=1)

            @pl.when(first)
            def _(dw_ref=dw_ref, dw_acc=dw_acc):
                dw_ref[...] = dw_acc

            @pl.when(jnp.logical_not(first))
            def _(dw_ref=dw_ref, dw_acc=dw_acc):
                dw_ref[...] += dw_acc
        out_ref[:, 2 * width: 3 * width] = dv_ref[...].astype(BF16)
        out_ref[:, 3 * width: 3 * width + ch] = da_ref[...]
        out_ref[:, 3 * width + ch: 3 * width + 2 * ch] = db_ref[...]
        out_ref[:, 3 * width + 2 * ch: out_w] = dgl_ref[...]

    vec = pl.BlockSpec((1, LANES), lambda i: (0, 0))
    blk = lambda c: pl.BlockSpec((tm, width), lambda i: (i, c))
    cblk = pl.BlockSpec((tm, ch), lambda i: (i, 0))
    return _pallas(
        body, name=name, grid=(s // tm,),
        in_specs=[blk(0), blk(0), blk(0), cblk, cblk, pl.BlockSpec((tm, gate_w), lambda i: (i, 0)),
                  blk(0), blk(1), pl.BlockSpec((tm, 1), lambda i: (i, 0)), vec, vec, vec, vec, vec,
                  pl.BlockSpec((LANES, LANES), lambda i: (0, 0))],
        out_specs=[pl.BlockSpec((tm, out_w), lambda i: (i, 0)), vec, vec],
        out_shape=[jax.ShapeDtypeStruct((s, out_w), BF16)] + [jax.ShapeDtypeStruct((1, LANES), F32)] * 2,
        operands=[dqn, dkn, dv, da, db, dgl, proj, proj, pos_col, qw2, kw2, invf, m_a, m_b, bd],
        rider=rider)


def _row_chunks(n_rows, fn, chunk=256):
    def step(i, c):
        fn(pl.ds(pl.multiple_of(i * chunk, chunk), chunk))
        return c
    lax.fori_loop(0, n_rows // chunk, step, 0)


def _to_residue_major(dst, src, s, d, dst_off=0, cast=None):
    seq = s // d
    for r in range(d):
        v = src[...] if d == 1 else src[pl.ds(r, seq, stride=d), :]
        dst[dst_off + r * seq: dst_off + (r + 1) * seq, :] = v if cast is None else v.astype(cast)


def _from_residue_major(dst, src, s, d, src_off=0):
    seq = s // d
    for r in range(d):
        v = src[src_off + r * seq: src_off + (r + 1) * seq, :]
        if d == 1:
            dst[...] = v
        else:
            dst[pl.ds(r, seq, stride=d), :] = v


def _band_bias():
    qi = lax.broadcasted_iota(jnp.int32, (QBLK, KWIN), 0)
    kj = lax.broadcasted_iota(jnp.int32, (QBLK, KWIN), 1)
    return jnp.where(jnp.abs(kj - HALF_SPAN - qi) <= HALF_SPAN, 0.0, NEG_INF).astype(F32)


def _range_bias(base, seq):
    kj = lax.broadcasted_iota(jnp.int32, (1, KWIN), 1)
    lo = (base & -seq) - base + HALF_SPAN
    return jnp.where((kj >= lo) & (kj < lo + seq), 0.0, NEG_INF).astype(F32)


def _skewed_blocks(n_blk, produce, consume):
    produce(0, 0)
    for b in range(n_blk):
        consume(b, b % 2)
        if b + 1 < n_blk:
            produce(b + 1, (b + 1) % 2)


def _block_base(b):
    return b * QBLK if isinstance(b, int) else pl.multiple_of(b * QBLK, QBLK)


def _attn_fwd(qn, kn, proj, name, rider=None):
    s = qn.shape[0]
    n_pairs = N_SLOT_HEADS * HEAD_DIM // LANES
    v_col0 = 2 * qn.shape[1] // LANES
    nt_dims = (((1,), (1,)), ((), ()))

    def body(q_ref, k_ref, v_ref, attn_ref, lse_ref, attn_b_ref, q_rm, k_rm, v_rm, acc_rm, m_rm, l_rm,
             acc_p, m_p, l_p, m_run, l_run, acc_run, band, s_buf, m_buf):
        g = pl.program_id(1)
        zpad = jnp.zeros((HALF_SPAN, LANES), BF16)
        k_rm[0:HALF_SPAN, :] = zpad
        k_rm[s + HALF_SPAN: s + 2 * HALF_SPAN, :] = zpad
        v_rm[0:HALF_SPAN, 0:LANES] = zpad
        v_rm[s + HALF_SPAN: s + 2 * HALF_SPAN, 0:LANES] = zpad

        def ones_rows(rows):
            v_rm[pl.ds(rows.start, rows.size), LANES:2 * LANES] = jnp.ones((rows.size, LANES), BF16)

        _row_chunks(s + 2 * HALF_SPAN, ones_rows, chunk=2 * HALF_SPAN)
        band[...] = _band_bias()
        lane = lax.broadcasted_iota(jnp.int32, (QBLK, LANES), 1)
        low = lane < HEAD_DIM
        n_blk = s // QBLK

        for gi, d in enumerate(DILATIONS):
            @pl.when(g == gi)
            def _(gi=gi, d=d):
                seq = s // d
                _to_residue_major(q_rm, q_ref, s, d, cast=BF16)
                _to_residue_major(k_rm, k_ref, s, d, dst_off=HALF_SPAN, cast=BF16)
                _to_residue_major(v_rm.at[:, 0:LANES], v_ref, s, d, dst_off=HALF_SPAN, cast=BF16)

                def scores(b, slot):
                    base = _block_base(b)
                    q = q_rm[pl.ds(base, QBLK), :]
                    zero = jnp.zeros_like(q)
                    q2 = jnp.concatenate([jnp.where(low, q, zero), jnp.where(low, zero, q)], axis=0)
                    sc = lax.dot_general(q2, k_rm[pl.ds(base, KWIN), :], nt_dims, preferred_element_type=F32)
                    bias = band[...] + _range_bias(base, seq)
                    for hh in range(2):
                        rows = slice(hh * QBLK, (hh + 1) * QBLK)
                        sh = sc[rows, :] + bias
                        s_buf[slot, rows, :] = sh
                        m_buf[slot, rows, :] = jnp.broadcast_to(jnp.max(sh, axis=-1, keepdims=True), (QBLK, LANES))

                def outputs(b, slot):
                    base = _block_base(b)
                    sv = s_buf[slot]
                    mb = m_buf[slot]
                    p = jnp.exp(jnp.concatenate([sv[:, 0:LANES] - mb, sv[:, LANES:2 * LANES] - mb], axis=1))
                    pv = jnp.dot(p.astype(BF16), v_rm[pl.ds(base, KWIN), :], preferred_element_type=F32)
                    rows = pl.ds(base, QBLK)
                    acc_rm[rows, :] = jnp.where(low, pv[0:QBLK, 0:LANES], pv[QBLK:2 * QBLK, 0:LANES])
                    l_rm[rows, :] = jnp.where(low, pv[0:QBLK, LANES:2 * LANES], pv[QBLK:2 * QBLK, LANES:2 * LANES])
                    m_rm[rows, :] = jnp.where(low, mb[0:QBLK, :], mb[QBLK:2 * QBLK, :])

                _skewed_blocks(n_blk, scores, outputs)
                if d == 1:
                    src = (acc_rm, m_rm, l_rm)
                else:
                    for dst_, src_ in ((acc_p, acc_rm), (m_p, m_rm), (l_p, l_rm)):
                        _from_residue_major(dst_, src_, s, d)
                    src = (acc_p, m_p, l_p)

                def combine(rows):
                    a_g, m_g, l_g = src[0][rows, :], src[1][rows, :], src[2][rows, :]
                    if gi == 0:
                        m_new, l_new, a_new = m_g, l_g, a_g
                    else:
                        m_old = m_run[rows, :]
                        m_new = jnp.maximum(m_old, m_g)
                        w_old = jnp.exp(m_old - m_new)
                        w_g = jnp.exp(m_g - m_new)
                        l_new = l_run[rows, :] * w_old + l_g * w_g
                        a_new = acc_run[rows, :] * w_old + a_g * w_g
                    if gi == len(DILATIONS) - 1:
                        out = a_new / l_new
                        attn_ref[rows, :] = out
                        attn_b_ref[rows, :] = out.astype(BF16)
                        lse_ref[rows, :] = m_new + jnp.log(l_new)
                    else:
                        m_run[rows, :] = m_new
                        l_run[rows, :] = l_new
                        acc_run[rows, :] = a_new

                _row_chunks(s, combine)

    qk_spec = pl.BlockSpec((s, LANES), lambda hp, g: (0, g * n_pairs + hp))
    v_spec = pl.BlockSpec((s, LANES), lambda hp, g: (0, v_col0 + g * n_pairs + hp))
    o_spec = pl.BlockSpec((s, LANES), lambda hp, g: (0, hp))
    f32buf = pltpu.VMEM((s, LANES), F32)
    return _pallas(
        body, name=name, grid=(n_pairs, len(DILATIONS)), in_specs=[qk_spec, qk_spec, v_spec],
        out_specs=[o_spec, o_spec, o_spec],
        out_shape=[jax.ShapeDtypeStruct((s, n_pairs * LANES), F32)] * 2
        + [jax.ShapeDtypeStruct((s, n_pairs * LANES), BF16)],
        operands=[qn, kn, proj],
        scratch_shapes=[pltpu.VMEM((s, LANES), BF16), pltpu.VMEM((s + 2 * HALF_SPAN, LANES), BF16),
                        pltpu.VMEM((s + 2 * HALF_SPAN, 2 * LANES), BF16)] + [f32buf] * 9
        + [pltpu.VMEM((QBLK, KWIN), F32), pltpu.VMEM((2, 2 * QBLK, KWIN), F32),
           pltpu.VMEM((2, 2 * QBLK, LANES), F32)],
        rider=rider)


def _attn_bwd(qn, kn, proj, dattn, attn, lse, bd, name, rider=None):
    s = qn.shape[0]
    n_pairs = N_SLOT_HEADS * HEAD_DIM // LANES
    v_col0 = 2 * qn.shape[1] // LANES
    nt_dims = (((1,), (1,)), ((), ()))
    tn_dims = (((0,), (0,)), ((), ()))
    spad = s + 2 * HALF_SPAN

    def body(q_ref, k_ref, v_ref, do_ref, o_ref, lse_ref, bd_ref, dq_ref, dk_ref, dv_ref,
             q_rm, k_rm, v_rm, do_rm, lse0_rm, lse1_rm, dd0_rm, dd1_rm, dq_rm, dk_rm, dv_rm,
             lse0_p, lse1_p, dd0_p, dd1_p, band, p_buf, ds_buf):
        g = pl.program_id(1)
        zpad = jnp.zeros((HALF_SPAN, LANES), BF16)
        for buf in (k_rm, v_rm):
            buf[0:HALF_SPAN, :] = zpad
            buf[s + HALF_SPAN: spad, :] = zpad
        zf = jnp.zeros((HALF_SPAN, LANES), F32)
        for buf in (dk_rm, dv_rm):
            buf[0:HALF_SPAN, :] = zf
            buf[s + HALF_SPAN: spad, :] = zf
        band[...] = _band_bias()

        def clear(rows):
            z = jnp.zeros((rows.size, LANES), F32)
            dk_rm[pl.ds(rows.start + HALF_SPAN, rows.size), :] = z
            dv_rm[pl.ds(rows.start + HALF_SPAN, rows.size), :] = z

        _row_chunks(s, clear)

        def prepare(rows):
            lo = lax.broadcasted_iota(jnp.int32, (rows.size, LANES), 1) < HEAD_DIM
            dsum = _head_sums(do_ref[rows, :] * o_ref[rows, :], bd_ref[...])
            dswap = pltpu.roll(dsum, HEAD_DIM, axis=1)
            dd0_p[rows, :] = jnp.where(lo, dsum, dswap)
            dd1_p[rows, :] = jnp.where(lo, dswap, dsum)
            lv = lse_ref[rows, :]
            lswap = pltpu.roll(lv, HEAD_DIM, axis=1)
            lse0_p[rows, :] = jnp.where(lo, lv, lswap)
            lse1_p[rows, :] = jnp.where(lo, lswap, lv)

        @pl.when(g == 0)
        def _():
            _row_chunks(s, prepare)
        lane = lax.broadcasted_iota(jnp.int32, (QBLK, LANES), 1)
        low = lane < HEAD_DIM
        n_blk = s // QBLK

        def stacked(ref, rows):
            val = ref[rows, :]
            zero = jnp.zeros_like(val)
            return jnp.concatenate([jnp.where(low, val, zero), jnp.where(low, zero, val)], axis=0)

        for gi, d in enumerate(DILATIONS):
            @pl.when(g == gi)
            def _(d=d):
                seq = s // d
                _to_residue_major(q_rm, q_ref, s, d, cast=BF16)
                _to_residue_major(k_rm, k_ref, s, d, dst_off=HALF_SPAN, cast=BF16)
                _to_residue_major(v_rm, v_ref, s, d, dst_off=HALF_SPAN, cast=BF16)
                _to_residue_major(do_rm, do_ref, s, d, cast=BF16)
                for dst_, src_ in ((lse0_rm, lse0_p), (lse1_rm, lse1_p), (dd0_rm, dd0_p), (dd1_rm, dd1_p)):
                    _to_residue_major(dst_, src_, s, d)

                def scores(b, slot):
                    base = _block_base(b)
                    rows = pl.ds(base, QBLK)
                    win = pl.ds(base, KWIN)
                    sc = lax.dot_general(stacked(q_rm, rows), k_rm[win, :], nt_dims, preferred_element_type=F32)
                    dp = lax.dot_general(stacked(do_rm, rows), v_rm[win, :], nt_dims, preferred_element_type=F32)
                    bias = band[...] + _range_bias(base, seq)
                    for hh, (lse_r, dd_r) in enumerate(((lse0_rm, dd0_rm), (lse1_rm, dd1_rm))):
                        r = slice(hh * QBLK, (hh + 1) * QBLK)
                        lse_h = lse_r[rows, :]
                        dd_h = dd_r[rows, :]
                        sh = sc[r, :] + bias
                        p = jnp.exp(jnp.concatenate([sh[:, 0:LANES] - lse_h, sh[:, LANES:KWIN] - lse_h], axis=1))
                        dph = dp[r, :]
                        ds = p * jnp.concatenate([dph[:, 0:LANES] - dd_h, dph[:, LANES:KWIN] - dd_h], axis=1)
                        p_buf[slot, r, :] = p.astype(BF16)
                        ds_buf[slot, r, :] = ds.astype(BF16)

                def grads(b, slot):
                    base = _block_base(b)
                    rows = pl.ds(base, QBLK)
                    win = pl.ds(base, KWIN)
                    p = p_buf[slot]
                    ds = ds_buf[slot]
                    dq2 = jnp.dot(ds, k_rm[win, :], preferred_element_type=F32)
                    dq_rm[rows, :] = jnp.where(low, dq2[0:QBLK, :], dq2[QBLK:2 * QBLK, :])
                    dk_rm[win, :] += lax.dot_general(ds, stacked(q_rm, rows), tn_dims, preferred_element_type=F32)
                    dv_rm[win, :] += lax.dot_general(p, stacked(do_rm, rows), tn_dims, preferred_element_type=F32)

                _skewed_blocks(n_blk, scores, grads)
                _from_residue_major(dq_ref, dq_rm, s, d)
                _from_residue_major(dk_ref, dk_rm, s, d, src_off=HALF_SPAN)
                _from_residue_major(dv_ref, dv_rm, s, d, src_off=HALF_SPAN)

    qk_spec = pl.BlockSpec((s, LANES), lambda hp, g: (0, g * n_pairs + hp))
    v_spec = pl.BlockSpec((s, LANES), lambda hp, g: (0, v_col0 + g * n_pairs + hp))
    o_spec = pl.BlockSpec((s, LANES), lambda hp, g: (0, hp))
    width = qn.shape[1]
    f32buf = pltpu.VMEM((s, LANES), F32)
    f32pad = pltpu.VMEM((spad, LANES), F32)
    return _pallas(
        body, name=name, grid=(n_pairs, len(DILATIONS)),
        in_specs=[qk_spec, qk_spec, v_spec, o_spec, o_spec, o_spec,
                  pl.BlockSpec((LANES, LANES), lambda hp, g: (0, 0))],
        out_specs=[qk_spec, qk_spec, qk_spec],
        out_shape=[jax.ShapeDtypeStruct((s, width), F32)] * 3,
        operands=[qn, kn, proj, dattn, attn, lse, bd],
        scratch_shapes=[pltpu.VMEM((s, LANES), BF16), pltpu.VMEM((spad, LANES), BF16),
                        pltpu.VMEM((spad, LANES), BF16), pltpu.VMEM((s, LANES), BF16),
                        f32buf, f32buf, f32buf, f32buf, f32buf, f32pad, f32pad,
                        f32buf, f32buf, f32buf, f32buf, pltpu.VMEM((QBLK, KWIN), F32),
                        pltpu.VMEM((2, 2 * QBLK, KWIN), BF16), pltpu.VMEM((2, 2 * QBLK, KWIN), BF16)],
        rider=rider)


CONV_PAD = 16


def _conv_fwd(proj, conv_w, conv_b, col0, name, rider=None):
    s = proj.shape[0]
    ch = conv_w.shape[1]
    nblk = ch // LANES
    a0 = col0 // LANES
    tr = 256
    shift = CONV_PAD - (CONV_WIDTH - 1) // 2

    def body(a_ref, b_ref, w_ref, bias_ref, u0_ref, uc_ref, pad):
        z = jnp.zeros((CONV_PAD, LANES), F32)
        pad[0:CONV_PAD, :] = z
        pad[s + CONV_PAD: s + 2 * CONV_PAD, :] = z

        def glu(rows):
            u0 = a_ref[rows, :] * jax.nn.sigmoid(b_ref[rows, :])
            u0_ref[rows, :] = u0
            pad[pl.ds(rows.start + CONV_PAD, rows.size), :] = u0

        _row_chunks(s, glu)
        for t in range(0, s, tr):
            acc = jnp.broadcast_to(bias_ref[...], (tr, LANES))
            for k in range(CONV_WIDTH):
                acc = acc + w_ref[k:k + 1, :] * pad[t + k + shift: t + k + shift + tr, :]
            uc_ref[t:t + tr, :] = acc

    return _pallas(
        body, name=name, grid=(nblk,),
        in_specs=[pl.BlockSpec((s, LANES), lambda c: (0, a0 + c)),
                  pl.BlockSpec((s, LANES), lambda c: (0, a0 + nblk + c)),
                  pl.BlockSpec((CONV_WIDTH, LANES), lambda c: (0, c)),
                  pl.BlockSpec((1, LANES), lambda c: (0, c))],
        out_specs=[pl.BlockSpec((s, LANES), lambda c: (0, c))] * 2,
        out_shape=[jax.ShapeDtypeStruct((s, ch), F32)] * 2, operands=[proj, proj, conv_w, conv_b],
        scratch_shapes=[pltpu.VMEM((s + 2 * CONV_PAD, LANES), F32)], rider=rider)


def _ln_silu_fwd(uc, ln_w, ln_b, name):
    s, ch = uc.shape
    tm = 256

    def body(u_ref, w_ref, b_ref, o_ref):
        u = u_ref[...]
        mu = jnp.mean(u, axis=-1, keepdims=True)
        xc = u - mu
        rstd = lax.rsqrt(jnp.mean(xc * xc, axis=-1, keepdims=True) + EPS)
        z = xc * rstd * w_ref[...] + b_ref[...]
        o_ref[...] = (z * jax.nn.sigmoid(z)).astype(BF16)

    row = pl.BlockSpec((tm, ch), lambda i: (i, 0))
    vec = pl.BlockSpec((1, ch), lambda i: (0, 0))
    return pl.pallas_call(
        body, name=name, grid=(s // tm,), in_specs=[row, vec, vec], out_specs=row,
        out_shape=jax.ShapeDtypeStruct((s, ch), BF16), compiler_params=_params(),
    )(uc, ln_w, ln_b)


def _ln_silu_bwd(du3, uc, ln_w, ln_b, name):
    s, ch = uc.shape
    tm = 256

    def body(d_ref, u_ref, w_ref, b_ref, du_ref, dw_ref, db_ref):
        u = u_ref[...]
        mu = jnp.mean(u, axis=-1, keepdims=True)
        xc = u - mu
        rstd = lax.rsqrt(jnp.mean(xc * xc, axis=-1, keepdims=True) + EPS)
        xhat = xc * rstd
        z = xhat * w_ref[...] + b_ref[...]
        sg = jax.nn.sigmoid(z)
        dz = d_ref[...] * (sg * (1.0 + z * (1.0 - sg)))
        dxh = dz * w_ref[...]
        du_ref[...] = rstd * (dxh - jnp.mean(dxh, axis=-1, keepdims=True)
                              - xhat * jnp.mean(dxh * xhat, axis=-1, keepdims=True))
        pw = jnp.sum(dz * xhat, axis=0, keepdims=True)
        pb = jnp.sum(dz, axis=0, keepdims=True)
        first = pl.program_id(0) == 0

        @pl.when(first)
        def _():
            dw_ref[...] = pw
            db_ref[...] = pb

        @pl.when(jnp.logical_not(first))
        def _():
            dw_ref[...] += pw
            db_ref[...] += pb

    row = pl.BlockSpec((tm, ch), lambda i: (i, 0))
    vec = pl.BlockSpec((1, ch), lambda i: (0, 0))
    return pl.pallas_call(
        body, name=name, grid=(s // tm,), in_specs=[row, row, vec, vec], out_specs=[row, vec, vec],
        out_shape=[jax.ShapeDtypeStruct((s, ch), F32), jax.ShapeDtypeStruct((1, ch), F32),
                   jax.ShapeDtypeStruct((1, ch), F32)],
        compiler_params=_params(),
    )(du3, uc, ln_w, ln_b)


def _conv_bwd(duc, u0, proj, conv_w, col0, name, rider=None):
    s = proj.shape[0]
    ch = conv_w.shape[1]
    nblk = ch // LANES
    a0 = col0 // LANES
    tr = 256
    half = (CONV_WIDTH - 1) // 2
    shift = CONV_PAD - half

    def body(duc_ref, u0_ref, a_ref, b_ref, w_ref, da_ref, db_ref, dw_ref, dbias_ref, pad_d, pad_u):
        z = jnp.zeros((CONV_PAD, LANES), F32)
        for buf in (pad_d, pad_u):
            buf[0:CONV_PAD, :] = z
            buf[s + CONV_PAD: s + 2 * CONV_PAD, :] = z

        def fill(rows):
            dst = pl.ds(rows.start + CONV_PAD, rows.size)
            pad_d[dst, :] = duc_ref[rows, :]
            pad_u[dst, :] = u0_ref[rows, :]

        _row_chunks(s, fill)
        dw_acc = [jnp.zeros((8, LANES), F32) for _ in range(CONV_WIDTH)]
        dbias_acc = jnp.zeros((8, LANES), F32)
        for t in range(0, s, tr):
            d_t = duc_ref[t:t + tr, :]
            dbias_acc = dbias_acc + jnp.sum(d_t.reshape(tr // 8, 8, LANES), axis=0)
            du0 = jnp.zeros((tr, LANES), F32)
            for k in range(CONV_WIDTH):
                du0 = du0 + w_ref[k:k + 1, :] * pad_d[t - k + half + CONV_PAD: t - k + half + CONV_PAD + tr, :]
                prod = d_t * pad_u[t + k + shift: t + k + shift + tr, :]
                dw_acc[k] = dw_acc[k] + jnp.sum(prod.reshape(tr // 8, 8, LANES), axis=0)
            av = a_ref[t:t + tr, :]
            sg = jax.nn.sigmoid(b_ref[t:t + tr, :])
            da_ref[t:t + tr, :] = (du0 * sg).astype(BF16)
            db_ref[t:t + tr, :] = (du0 * av * sg * (1.0 - sg)).astype(BF16)
        for k in range(CONV_WIDTH):
            dw_ref[k:k + 1, :] = jnp.sum(dw_acc[k], axis=0, keepdims=True)
        dbias_ref[...] = jnp.sum(dbias_acc, axis=0, keepdims=True)

    col = lambda off: pl.BlockSpec((s, LANES), lambda c: (0, off + c))
    return _pallas(
        body, name=name, grid=(nblk,),
        in_specs=[col(0), col(0), col(a0), col(a0 + nblk),
                  pl.BlockSpec((CONV_WIDTH, LANES), lambda c: (0, c))],
        out_specs=[col(0), col(0), pl.BlockSpec((CONV_WIDTH, LANES), lambda c: (0, c)),
                   pl.BlockSpec((1, LANES), lambda c: (0, c))],
        out_shape=[jax.ShapeDtypeStruct((s, ch), BF16)] * 2
        + [jax.ShapeDtypeStruct((CONV_WIDTH, ch), F32), jax.ShapeDtypeStruct((1, ch), F32)],
        operands=[duc, u0, proj, proj, conv_w],
        scratch_shapes=[pltpu.VMEM((s + 2 * CONV_PAD, LANES), F32)] * 2, rider=rider)


def _mix_out_proj(attn_b, u3, w_o, w_pw, proj, bg, col0, w_out, x, norm_w, name, rider=None):
    s, d = x.shape
    k = attn_b.shape[1]
    tm = 256
    half = d // 2
    assert col0 % half == 0
    c0 = col0 // half

    def body(a_ref, u_ref, wo_ref, wp_ref, a0_ref, a1_ref, b0_ref, b1_ref, bias_ref, w_ref, x_ref, nw_ref,
             ya_ref, yb_ref, mixed_ref, x1_ref, h2_ref):
        mixed = None
        for br, (src_ref, wb_ref, lo_ref, hi_ref, y_ref) in enumerate(((a_ref, wo_ref, a0_ref, a1_ref, ya_ref),
                                                                       (u_ref, wp_ref, b0_ref, b1_ref, yb_ref))):
            src = src_ref[...]
            y = jnp.concatenate([jnp.dot(src, wb_ref[j], preferred_element_type=F32) for j in range(N_CHIPS)],
                                axis=1)
            y_ref[...] = y
            logits = jnp.concatenate([lo_ref[...], hi_ref[...]], axis=1)
            part = jax.nn.sigmoid(logits + bias_ref[br]) * y
            mixed = part if mixed is None else mixed + part
        mixed = mixed.astype(BF16)
        mixed_ref[...] = mixed
        x1 = x_ref[...] + jnp.dot(mixed, w_ref[...], preferred_element_type=F32)
        x1_ref[...] = x1
        rstd = lax.rsqrt(jnp.mean(x1 * x1, axis=-1, keepdims=True) + EPS)
        h2_ref[...] = (x1 * rstd * nw_ref[...]).astype(BF16)

    row = pl.BlockSpec((tm, d), lambda i: (i, 0))
    src_row = pl.BlockSpec((tm, k), lambda i: (i, 0))
    blocks = pl.BlockSpec(w_o.shape, lambda i: (0, 0, 0))
    logit_blk = lambda j: pl.BlockSpec((tm, half), functools.partial(lambda i, j: (i, c0 + j), j=j))
    return _pallas(
        body, name=name, grid=(s // tm,),
        in_specs=[src_row, src_row, blocks, blocks, logit_blk(0), logit_blk(1), logit_blk(2), logit_blk(3),
                  pl.BlockSpec((2, 1, d), lambda i: (0, 0, 0)), pl.BlockSpec((d, d), lambda i: (0, 0)), row,
                  pl.BlockSpec((1, d), lambda i: (0, 0))],
        out_specs=[row] * 5,
        out_shape=[jax.ShapeDtypeStruct((s, d), F32), jax.ShapeDtypeStruct((s, d), F32),
                   jax.ShapeDtypeStruct((s, d), BF16), jax.ShapeDtypeStruct((s, d), F32),
                   jax.ShapeDtypeStruct((s, d), BF16)],
        operands=[attn_b, u3, w_o, w_pw, proj, proj, proj, proj, bg, w_out, x, norm_w], rider=rider)


def _out_proj_bwd_gates(dx1, w_out, proj, bg, y_a, y_b, w_o, w_pw, col0, name, after=()):
    s, d = y_a.shape
    n_blk, k, blk = w_o.shape
    tm = 256
    half = d // 2
    assert col0 % half == 0
    c0 = col0 // half
    nt_dims = (((1,), (1,)), ((), ()))

    def body(dx_ref, w_ref, a0_ref, a1_ref, b0_ref, b1_ref, bias_ref, ya_ref, yb_ref, wo_ref, wp_ref,
             dgl_ref, dya_ref, dyb_ref, db_ref, da_ref, du_ref):
        dm = lax.dot_general(dx_ref[...], w_ref[...], nt_dims, preferred_element_type=F32)
        parts = []
        for br, (lo_ref, hi_ref, y_ref, dy_ref, wb_ref, dsrc_ref) in enumerate((
                (a0_ref, a1_ref, ya_ref, dya_ref, wo_ref, da_ref), (b0_ref, b1_ref, yb_ref, dyb_ref, wp_ref, du_ref))):
            logits = jnp.concatenate([lo_ref[...], hi_ref[...]], axis=1)
            gate = jax.nn.sigmoid(logits + bias_ref[br])
            dy = (dm * gate).astype(BF16)
            dy_ref[...] = dy
            dsrc = lax.dot_general(dy[:, 0:blk], wb_ref[0], nt_dims, preferred_element_type=F32)
            for j in range(1, n_blk):
                dsrc = dsrc + lax.dot_general(dy[:, j * blk:(j + 1) * blk], wb_ref[j], nt_dims,
                                              preferred_element_type=F32)
            dsrc_ref[...] = dsrc
            dgl = dm * y_ref[...] * gate * (1.0 - gate)
            dgl_ref[:, br * d:(br + 1) * d] = dgl.astype(BF16)
            parts.append(jnp.sum(dgl, axis=0, keepdims=True))
        part = jnp.concatenate(parts, axis=0)
        first = pl.program_id(0) == 0

        @pl.when(first)
        def _():
            db_ref[...] = part

        @pl.when(jnp.logical_not(first))
        def _():
            db_ref[...] += part

    row = pl.BlockSpec((tm, d), lambda i: (i, 0))
    logit_blk = lambda k: pl.BlockSpec((tm, half), functools.partial(lambda i, k: (i, c0 + k), k=k))
    blocks = pl.BlockSpec(w_o.shape, lambda i: (0, 0, 0))
    src_row = pl.BlockSpec((tm, k), lambda i: (i, 0))
    return _pallas(
        body, name=name, grid=(s // tm,),
        in_specs=[row, pl.BlockSpec((d, d), lambda i: (0, 0)), logit_blk(0), logit_blk(1), logit_blk(2),
                  logit_blk(3), pl.BlockSpec((2, 1, d), lambda i: (0, 0, 0)), row, row, blocks, blocks],
        out_specs=[pl.BlockSpec((tm, 2 * d), lambda i: (i, 0)), row, row, pl.BlockSpec((2, d), lambda i: (0, 0)),
                   src_row, src_row],
        out_shape=[jax.ShapeDtypeStruct((s, 2 * d), BF16), jax.ShapeDtypeStruct((s, d), BF16),
                   jax.ShapeDtypeStruct((s, d), BF16), jax.ShapeDtypeStruct((2, d), F32),
                   jax.ShapeDtypeStruct((s, k), F32), jax.ShapeDtypeStruct((s, k), F32)],
        operands=[dx1, w_out, proj, proj, proj, proj, bg, y_a, y_b, w_o, w_pw], after=after)


def _ffn_in_swiglu(h2, w_blocked, name):
    s, k = h2.shape
    nblk, _, tn = w_blocked.shape
    ff = nblk // 2 * tn
    tm = 512

    def body(a_ref, wg_ref, wu_ref, g_ref, u_ref, act_ref):
        a = a_ref[...]
        gt = jnp.dot(a, wg_ref[...], preferred_element_type=F32)
        up = jnp.dot(a, wu_ref[...], preferred_element_type=F32)
        g_ref[...] = gt
        u_ref[...] = up
        act_ref[...] = (gt * jax.nn.sigmoid(gt) * up).astype(BF16)

    out = pl.BlockSpec((tm, tn), lambda j, i: (i, j))
    return pl.pallas_call(
        body, name=name, grid=(nblk // 2, s // tm),
        in_specs=[pl.BlockSpec((tm, k), lambda j, i: (i, 0)),
                  pl.BlockSpec((None, k, tn), lambda j, i: (j, 0, 0)),
                  pl.BlockSpec((None, k, tn), lambda j, i: (nblk // 2 + j, 0, 0))],
        out_specs=[out, out, out],
        out_shape=[jax.ShapeDtypeStruct((s, ff), F32), jax.ShapeDtypeStruct((s, ff), F32),
                   jax.ShapeDtypeStruct((s, ff), BF16)],
        compiler_params=_params(),
    )(h2, w_blocked, w_blocked)


def _ffn_out_bwd_swiglu(dy, w_ffn_out, gate, up, name, rider=None):
    s, d = dy.shape
    ff = gate.shape[1]
    tm = 256
    nt_dims = (((1,), (1,)), ((), ()))

    def body(dy_ref, w_ref, g_ref, u_ref, o_ref):
        dv = lax.dot_general(dy_ref[...], w_ref[...], nt_dims, preferred_element_type=F32)
        gt = g_ref[...]
        sg = jax.nn.sigmoid(gt)
        o_ref[:, 0:ff] = (dv * u_ref[...] * (sg * (1.0 + gt * (1.0 - sg)))).astype(BF16)
        o_ref[:, ff:2 * ff] = (dv * gt * sg).astype(BF16)

    row = pl.BlockSpec((tm, ff), lambda i: (i, 0))
    return _pallas(
        body, name=name, grid=(s // tm,),
        in_specs=[pl.BlockSpec((tm, d), lambda i: (i, 0)), pl.BlockSpec((ff, d), lambda i: (0, 0)), row, row],
        out_specs=pl.BlockSpec((tm, 2 * ff), lambda i: (i, 0)),
        out_shape=jax.ShapeDtypeStruct((s, 2 * ff), BF16), operands=[dy, w_ffn_out, gate, up], rider=rider)


def _ffn_out_loss(act, w_ffn_out, x1, target, name):
    s, k = act.shape
    d = w_ffn_out.shape[1]
    tm = 512

    def body(a_ref, w_ref, x1_ref, t_ref, dy_ref, dyb_ref, loss_ref, acc):
        y = x1_ref[...] + jnp.dot(a_ref[...], w_ref[...], preferred_element_type=F32)
        diff = y - t_ref[...]
        dy = diff * (1.0 / d)
        dy_ref[...] = dy
        dyb_ref[...] = dy.astype(BF16)
        part = jnp.sum((diff * diff).reshape(tm // 8, 8, d), axis=0)
        i = pl.program_id(0)

        @pl.when(i == 0)
        def _():
            acc[...] = part

        @pl.when(i > 0)
        def _():
            acc[...] += part

        @pl.when(i == pl.num_programs(0) - 1)
        def _():
            loss_ref[...] = (0.5 / d) * jnp.sum(jnp.sum(acc[...], axis=1, keepdims=True), axis=0, keepdims=True)

    row = pl.BlockSpec((tm, d), lambda i: (i, 0))
    return pl.pallas_call(
        body, name=name, grid=(s // tm,),
        in_specs=[pl.BlockSpec((tm, k), lambda i: (i, 0)), pl.BlockSpec((k, d), lambda i: (0, 0)), row, row],
        out_specs=[row, row, pl.BlockSpec((1, 1), lambda i: (0, 0))],
        out_shape=[jax.ShapeDtypeStruct((s, d), F32), jax.ShapeDtypeStruct((s, d), BF16),
                   jax.ShapeDtypeStruct((1, 1), F32)],
        scratch_shapes=[pltpu.VMEM((8, d), F32)], compiler_params=_params(),
    )(act, w_ffn_out, x1, target)


LATE_GATHER = ("w_o_attn", "w_pw_conv", "w_out", "w_ffn_in", "w_ffn_out")
EARLY_REDUCE = LATE_GATHER


def _blocks_by_half(g):
    if g.ndim == 2:
        g = g.reshape(N_CHIPS, g.shape[0] // N_CHIPS, g.shape[1])
    return g.reshape(N_CHIPS, 2, g.shape[1] // 2, g.shape[2])


def _forward_backward(x, pos_col, target, wts, first_gather, late_bufs, pos_arr):
    wts = dict(wts)
    consts = _rope_consts()
    bd = consts[3]
    qw2 = jnp.tile(wts["q_norm_w"], (1, LANES // HEAD_DIM))
    kw2 = jnp.tile(wts["k_norm_w"], (1, LANES // HEAD_DIM))
    qkv_w = 3 * N_SLOT_HEADS * HEAD_DIM
    conv_col0 = 3 * qkv_w

    h = _rmsnorm_fwd(x, wts["norm1_w"], "rms1_fwd")
    slot_order = jnp.bitwise_xor(pos_arr[1], jnp.asarray([0, 2, 1, 3], jnp.int32))
    blocked = lambda buf: buf.reshape(N_CHIPS, -1, buf.shape[3])
    near = first_gather
    proj = _proj_by_slot(h, blocked(near[2][0]), slot_order, 0, 1, "mm_proj_own", after=list(late_bufs))
    near = _split_middle(near, after=[proj], name="allgather_w_in_near_forward")
    far, _ = _split_start(_gather_both_legs_rider(near[2][:1], near[2][1:], FAR_CHIPS), "allgather_w_in_far_start")
    _, bufs = _split_wait((near[0], near[1], far[2], near[3]), after=[], name="allgather_w_in_near_wait")
    proj = _proj_by_slot(h, blocked(bufs[0]), slot_order, 1, len(NEAR_CHIPS), "mm_proj_near", proj=proj)
    far = _split_middle((far[0], far[1], bufs, far[3]), after=[proj], name="allgather_w_in_far_forward")
    (w_in_buf, conv_w_buf, b_gate_buf), _ = _split_wait(far, after=[], name="allgather_w_in_far_wait")
    wts["w_in"] = w_in_buf.reshape(N_CHIPS, -1, w_in_buf.shape[3])
    wts["conv_w"] = conv_w_buf.transpose(1, 0, 2).reshape(CONV_WIDTH, -1)
    wts["b_gate"] = b_gate_buf.transpose(1, 0, 2).reshape(2, 1, -1)
    ch = wts["conv_w"].shape[1]
    gate_col0 = conv_col0 + 2 * ch
    n_mix = LATE_GATHER.index("w_ffn_in")
    mix_gather, started = _split_start(_gather_ici_rider(late_bufs[:n_mix], []), "late_gather_mix_start",
                                       after=[wts["w_in"]])
    ffn_gather, started = _split_start(_gather_ici_rider(late_bufs[n_mix:], []), "late_gather_ffn_start",
                                       after=[started])
    proj = _proj_by_slot(h, wts["w_in"], slot_order, 1 + len(NEAR_CHIPS), len(FAR_CHIPS), "mm_proj_far", proj=proj,
                         after=[started])
    qn, kn = _qk_fwd(proj, pos_col, qw2, kw2, consts, "qk_fwd")
    attn, lse, attn_b = _attn_fwd(qn, kn, proj, "attn_fwd")

    def gathered(names, bufs):
        for n, buf in zip(names, bufs):
            full = buf.reshape(N_CHIPS, -1, buf.shape[3])
            wts[n] = full.reshape(-1, full.shape[2]) if n in ROW_SHARDED else full

    mix_bufs, _ = _split_wait(mix_gather, after=[attn_b], name="late_gather_mix_wait")
    (u0, uc), mix_bufs = _conv_fwd(proj, wts["conv_w"], wts["conv_b"], conv_col0, "conv_fwd",
                                   rider=_gather_forward_rider(mix_bufs))
    gathered(LATE_GATHER[:n_mix], mix_bufs)
    u3 = _ln_silu_fwd(uc, wts["conv_ln_w"], wts["conv_ln_b"], "ln_fwd")
    ffn_bufs, _ = _split_wait(ffn_gather, after=[u3], name="late_gather_ffn_wait")
    (y_a, y_b, mixed, x1, h2), ffn_bufs = _mix_out_proj(
        attn_b, u3, wts["w_o_attn"], wts["w_pw_conv"], proj, wts["b_gate"], gate_col0, wts["w_out"], x,
        wts["norm2_w"], "mix_x1_rms2", rider=_gather_forward_rider(ffn_bufs))
    gathered(LATE_GATHER[n_mix:], ffn_bufs)
    gate, up, act = _ffn_in_swiglu(h2, wts["w_ffn_in"], "mm_gu_swiglu")
    dy, dy_b16, loss = _ffn_out_loss(act, wts["w_ffn_out"], x1, target, "mm_x2_loss")

    g = {}
    by_chip = {}

    def pair_add(n, blocks, received):
        return _add_own_half(blocks, received, pos_arr, f"grads_pair_add_{n}")

    g_ffn_out = _blocks_by_half(
        _matmul(act, dy_b16, mode="tn", tm=1408, tn=1024, tk=2048, out_dtype=F32, name="mm_dwffnout"))
    dgu, (received,) = _ffn_out_bwd_swiglu(dy_b16, wts["w_ffn_out"], gate, up, "mm_dact_swiglu_bwd",
                                           rider=_pair_exchange_rider([g_ffn_out], halved=True))
    to_send, own = pair_add("w_ffn_out", g_ffn_out, received)
    dh2, (by_chip["w_ffn_out"],) = _matmul(
        dgu, wts["w_ffn_in"], mode="nt", tm=1024, tn=1024, tk=1408, out_dtype=F32, name="mm_dh2", b_blocked=True,
        rider=_chip_exchange_rider([to_send], [own]))
    g_ffn_in = _blocks_by_half(_matmul(h2, dgu, mode="tn", tm=512, tn=1408, tk=2048, out_dtype=F32,
                                       name="mm_dwffnin", out_blocked=N_CHIPS, cols_outer=True))
    exchanging, started = _split_start(_pair_exchange_rider([g_ffn_in], halved=True), "grads_ffn_in_pair_start")
    dx1, dx1_b16, g["norm2_w"] = _rmsnorm_bwd(dh2, x1, wts["norm2_w"], dy, "rms2_bwd")
    g["w_out"] = _matmul(mixed, dx1_b16, mode="tn", tm=512, tn=1024, tk=2048, out_dtype=F32, name="mm_dwout",
                         after=[started])
    dgl, dy_a, dy_b, g["b_gate"], dattn, du3 = _out_proj_bwd_gates(
        dx1_b16, wts["w_out"], proj, wts["b_gate"], y_a, y_b, wts["w_o_attn"], wts["w_pw_conv"], gate_col0,
        "mix_bwd")
    (received,), (g_ffn_in,) = _split_wait(exchanging, after=[dgl], name="grads_ffn_in_pair_wait")
    ffn_in_to_send, ffn_in_own = pair_add("w_ffn_in", g_ffn_in, received)
    g["w_o_attn"] = _matmul(attn_b, dy_a, mode="tn", tm=512, tn=256, tk=2048, out_dtype=F32, name="mm_dwo",
                            out_blocked=N_CHIPS)
    g["w_pw_conv"] = _matmul(u3, dy_b, mode="tn", tm=512, tn=256, tk=2048, out_dtype=F32, name="mm_dwpw",
                             out_blocked=N_CHIPS)
    duc, g["conv_ln_w"], g["conv_ln_b"] = _ln_silu_bwd(du3, uc, wts["conv_ln_w"], wts["conv_ln_b"], "ln_bwd")

    small3 = ("w_out", "w_o_attn", "w_pw_conv")
    g_small3 = [_blocks_by_half(g.pop(n)) for n in small3]
    (da, db, g["conv_w"], g["conv_b"]), received = _conv_bwd(
        duc, u0, proj, wts["conv_w"], conv_col0, "conv_bwd", rider=_pair_exchange_rider(g_small3, halved=True))
    sums3 = [pair_add(n, gb, rv) for n, gb, rv in zip(small3, g_small3, received)]
    (dqn, dkn, dv), (by_chip["w_ffn_in"],) = _attn_bwd(
        qn, kn, proj, dattn, attn, lse, bd, "attn_bwd",
        rider=_chip_exchange_rider([ffn_in_to_send], [ffn_in_own]))
    (dproj, dqw, dkw), exchanged3 = _qk_bwd(
        dqn, dkn, dv, da, db, dgl, proj, pos_col, qw2, kw2, consts, "qk_bwd",
        rider=_chip_exchange_rider([s[0] for s in sums3], [s[1] for s in sums3]))
    by_chip.update(zip(small3, exchanged3))
    halves = [_sum_chips(by_chip[n], pos_arr, f"grads_chip_sum_{n}") for n in EARLY_REDUCE]
    g["q_norm_w"] = dqw[:, :HEAD_DIM]
    g["k_norm_w"] = dkw[:, :HEAD_DIM]

    c = pos_arr[0]
    rh = h.shape[1] // 2
    h_sibling = lax.dynamic_slice_in_dim(h, (1 - c) * rh, rh, axis=1)
    h_own = lax.dynamic_slice_in_dim(h, c * rh, rh, axis=1)
    g_sibling, shards = _matmul(h_sibling, dproj, mode="tn", tm=rh, tn=1920, tk=2048, out_dtype=F32,
                                name="mm_dwin_sibling", out_blocked=N_CHIPS, rider=_pair_gather_rider(halves))
    reduced = dict(zip(EARLY_REDUCE, shards))
    exchanging, started = _split_start(_pair_exchange_rider([g_sibling], halved=False), "grads_w_in_pair_start")
    g_own = _matmul(h_own, dproj, mode="tn", tm=rh, tn=1920, tk=2048, out_dtype=F32, name="mm_dwin_own",
                    out_blocked=N_CHIPS, after=[started])
    (from_sibling,), _ = _split_wait(exchanging, after=[g_own], name="grads_w_in_pair_wait")
    to_send, own = _add_own_half(g_own, from_sibling, pos_arr, "grads_pair_add_w_in")
    in_flight, started = _split_start(_chip_exchange_rider([to_send], [own]), "grads_w_in_exchange_start")
    dh = _matmul(dproj, wts["w_in"], mode="nt", tm=1024, tn=1024, tk=1920, out_dtype=F32, name="mm_dh",
                 b_blocked=True, after=[started])
    grad_x, _, g["norm1_w"] = _rmsnorm_bwd(dh, x, wts["norm1_w"], dx1, "rms1_bwd")
    return loss, grad_x, g, reduced, (in_flight, started)


def _mesh_pos():
    return lax.axis_index("x"), lax.axis_index("y"), lax.axis_index("c")


def _other_chips(x, y):
    return [(1 - x, y), (x, 1 - y), (1 - x, 1 - y)]


NEAR_CHIPS, FAR_CHIPS = (0, 1), (2,)


def _cast_into_slot(shard, chip_arr, dtype, name, n_slots=N_CHIPS, after=()):
    r, c = shard.shape
    tr = r // 2 if r % 32 == 0 else r

    def body(chip_ref, s_ref, *refs):
        refs[-1][...] = s_ref[...].astype(dtype)

    return pl.pallas_call(
        body, name=name,
        grid_spec=pltpu.PrefetchScalarGridSpec(
            num_scalar_prefetch=1, grid=(r // tr,),
            in_specs=[pl.BlockSpec((tr, c), lambda i, chip_ref: (i, 0))] + [ANY] * len(after),
            out_specs=pl.BlockSpec((None, tr, c), lambda i, chip_ref: (chip_ref[0], i, 0))),
        out_shape=jax.ShapeDtypeStruct((n_slots, r, c), dtype), compiler_params=_params(),
    )(chip_arr, shard, *after)


GATHER_CHUNKS = 4


def _gather_both_legs_rider(big, small, peers):
    nb = len(big)
    n = nb + len(small)
    nch = GATHER_CHUNKS

    def part(bufs, a, slot, half, ch):
        if a >= nb:
            return bufs[a].at[slot]
        rows = bufs[a].shape[2] // nch
        return bufs[a].at[slot, half, pl.ds(ch * rows, rows)]

    def pieces():
        return [(a, ch, k) for ch in range(nch) for a in range(n) for k in peers if a < nb or ch == 0]

    def ici(bufs, sems, a, ch, k, slot_of_src):
        x, y, c = _mesh_pos()
        px, py = _other_chips(x, y)[k]
        slot = 2 * x + y if slot_of_src == "mine" else 2 * px + py
        return pltpu.make_async_remote_copy(
            src_ref=part(bufs, a, slot, c, ch), dst_ref=part(bufs, a, slot, c, ch), send_sem=sems[0].at[a, ch, k],
            recv_sem=sems[1].at[a, ch, k], device_id=(px, py, c), device_id_type=MESH)

    def forward(bufs, sems, a, ch, k, half):
        x, y, c = _mesh_pos()
        px, py = _other_chips(x, y)[k]
        h = c if half == "mine" else 1 - c
        return pltpu.make_async_remote_copy(
            src_ref=part(bufs, a, 2 * px + py, h, ch), dst_ref=part(bufs, a, 2 * px + py, h, ch),
            send_sem=sems[2].at[a, ch, k], recv_sem=sems[3].at[a, ch, k], device_id=(x, y, 1 - c),
            device_id_type=MESH)

    def start(r_in, bufs, sems):
        for a, ch, k in pieces():
            ici(bufs, sems, a, ch, k, "mine").start()

    def middle(r_in, bufs, sems):
        for a, ch, k in pieces():
            ici(bufs, sems, a, ch, k, "theirs").wait_recv()
            if a < nb:
                forward(bufs, sems, a, ch, k, "mine").start()
        for a, ch, k in pieces():
            ici(bufs, sems, a, ch, k, "mine").wait_send()

    def wait(r_in, bufs, sems):
        for a, ch, k in pieces():
            if a < nb:
                forward(bufs, sems, a, ch, k, "theirs").wait_recv()
        for a, ch, k in pieces():
            if a < nb:
                forward(bufs, sems, a, ch, k, "mine").wait_send()

    ops = list(big) + list(small)
    return _Rider(ops, [jax.ShapeDtypeStruct(o.shape, o.dtype) for o in ops], {i: i for i in range(n)},
                  [pltpu.SemaphoreType.DMA((n, nch, 3)), pltpu.SemaphoreType.DMA((n, nch, 3)),
                   pltpu.SemaphoreType.DMA((nb, nch, 3)), pltpu.SemaphoreType.DMA((nb, nch, 3))],
                  start, wait, middle)


def _comm_call(rider, name):
    def body():
        pass

    return _pallas(body, name=name, grid=(1,), in_specs=[], out_specs=[], out_shape=[], operands=[],
                   rider=rider)[1]


def _gather_ici_rider(big, small):
    nb = len(big)
    n = nb + len(small)

    def copies(bufs, sems):
        x, y, c = _mesh_pos()
        me = 2 * x + y
        part = lambda a, slot: bufs[a].at[slot, c] if a < nb else bufs[a].at[slot]
        out = []
        for a in range(n):
            for k, (px, py) in enumerate(_other_chips(x, y)):
                send = functools.partial(
                    pltpu.make_async_remote_copy,
                    src_ref=part(a, me), dst_ref=part(a, me), send_sem=sems[0].at[a, k],
                    recv_sem=sems[1].at[a, k], device_id=(px, py, c), device_id_type=MESH)
                recv = functools.partial(
                    pltpu.make_async_remote_copy,
                    src_ref=part(a, 2 * px + py), dst_ref=part(a, 2 * px + py), send_sem=sems[0].at[a, k],
                    recv_sem=sems[1].at[a, k], device_id=(px, py, c), device_id_type=MESH)
                out.append((send, recv))
        return out

    def start(r_in, r_out, sems):
        for send, _ in copies(r_out, sems):
            send().start()

    def wait(r_in, r_out, sems):
        cps = copies(r_out, sems)
        for _, recv in cps:
            recv().wait_recv()
        for send, _ in cps:
            send().wait_send()

    ops = list(big) + list(small)
    return _Rider(ops, [jax.ShapeDtypeStruct(o.shape, o.dtype) for o in ops], {i: i for i in range(n)},
                  [pltpu.SemaphoreType.DMA((n, 3)), pltpu.SemaphoreType.DMA((n, 3))], start, wait)


def _gather_forward_rider(big):
    n = len(big)

    def copies(bufs, sems):
        x, y, c = _mesh_pos()
        out = []
        for a in range(n):
            for k, (px, py) in enumerate(_other_chips(x, y)):
                slot = 2 * px + py
                send = functools.partial(
                    pltpu.make_async_remote_copy,
                    src_ref=bufs[a].at[slot, c], dst_ref=bufs[a].at[slot, c], send_sem=sems[0].at[a, k],
                    recv_sem=sems[1].at[a, k], device_id=(x, y, 1 - c), device_id_type=MESH)
                recv = functools.partial(
                    pltpu.make_async_remote_copy,
                    src_ref=bufs[a].at[slot, 1 - c], dst_ref=bufs[a].at[slot, 1 - c], send_sem=sems[0].at[a, k],
                    recv_sem=sems[1].at[a, k], device_id=(x, y, 1 - c), device_id_type=MESH)
                out.append((send, recv))
        return out

    def start(r_in, r_out, sems):
        for send, _ in copies(r_out, sems):
            send().start()

    def wait(r_in, r_out, sems):
        cps = copies(r_out, sems)
        for _, recv in cps:
            recv().wait_recv()
        for send, _ in cps:
            send().wait_send()

    return _Rider(big, [jax.ShapeDtypeStruct(o.shape, o.dtype) for o in big], {i: i for i in range(n)},
                  [pltpu.SemaphoreType.DMA((n, 3)), pltpu.SemaphoreType.DMA((n, 3))], start, wait)


def _pair_exchange_rider(gs, halved):
    n = len(gs)

    def copies(r_in, r_out, sems):
        x, y, c = _mesh_pos()
        return [pltpu.make_async_remote_copy(
            src_ref=r_in[a].at[:, 1 - c] if halved else r_in[a], dst_ref=r_out[a], send_sem=sems[0].at[a],
            recv_sem=sems[1].at[a], device_id=(x, y, 1 - c), device_id_type=MESH) for a in range(n)]

    def start(r_in, r_out, sems):
        for cp in copies(r_in, r_out, sems):
            cp.start()

    def wait(r_in, r_out, sems):
        for cp in copies(r_in, r_out, sems):
            cp.wait()

    return _Rider(gs, [jax.ShapeDtypeStruct((g.shape[0],) + g.shape[-2:], g.dtype) for g in gs], {},
                  [pltpu.SemaphoreType.DMA((n,)), pltpu.SemaphoreType.DMA((n,))], start, wait)


def _chip_exchange_rider(to_send, by_chip, row_range=None):
    n = len(to_send)

    def copies(r_in, r_out, sems):
        x, y, c = _mesh_pos()
        me = 2 * x + y
        rows = (lambda ref: ref) if row_range is None else (lambda ref: ref.at[pl.ds(*row_range)])
        out = []
        for a in range(n):
            for k, (px, py) in enumerate(_other_chips(x, y)):
                send = functools.partial(
                    pltpu.make_async_remote_copy,
                    src_ref=rows(r_in[a].at[2 * px + py]), dst_ref=rows(r_out[a].at[me]),
                    send_sem=sems[0].at[a, k], recv_sem=sems[1].at[a, k], device_id=(px, py, c),
                    device_id_type=MESH)
                recv = functools.partial(
                    pltpu.make_async_remote_copy,
                    src_ref=rows(r_in[a].at[me]), dst_ref=rows(r_out[a].at[2 * px + py]),
                    send_sem=sems[0].at[a, k], recv_sem=sems[1].at[a, k], device_id=(px, py, c),
                    device_id_type=MESH)
                out.append((send, recv))
        return out

    def start(r_in, r_out, sems):
        for send, _ in copies(r_in, r_out, sems):
            send().start()

    def wait(r_in, r_out, sems):
        cps = copies(r_in, r_out, sems)
        for _, recv in cps:
            recv().wait_recv()
        for send, _ in cps:
            send().wait_send()

    return _Rider(list(to_send) + list(by_chip), [jax.ShapeDtypeStruct(b.shape, b.dtype) for b in by_chip],
                  {n + i: i for i in range(n)},
                  [pltpu.SemaphoreType.DMA((n, 3)), pltpu.SemaphoreType.DMA((n, 3))], start, wait)


HBM = pl.BlockSpec(memory_space=pltpu.HBM)
SEM = pl.BlockSpec(memory_space=pltpu.SEMAPHORE)


_IN_FLIGHT = pltpu.CompilerParams(has_side_effects=pltpu.SideEffectType.DATAFLOW_SIDE_EFFECTING)


class _FlatSems:
    def __init__(self, ref, shape):
        self.ref, self.shape = ref, shape

    @property
    def at(self):
        return self

    def __getitem__(self, idx):
        idx = idx if isinstance(idx, tuple) else (idx,)
        flat = 0
        for i, n in zip(idx, self.shape):
            flat = flat * n + i
        return self.ref.at[flat]


def _flat_sem_types(rider):
    return tuple(pltpu.SemaphoreType.DMA((int(np.prod(s.shape)),)) for s in rider.scratch)


def _as_rider_sems(rider, refs):
    return [_FlatSems(r, s.shape) for r, s in zip(refs, rider.scratch)]


def _split_start(rider, name, after=()):
    n_in, n_out, n_sem = len(rider.operands), len(rider.out_shapes), len(rider.scratch)
    n_after = len(after)
    fresh = [j for j in range(n_out) if j not in rider.aliases.values()]
    by_out = {j: i for i, j in rider.aliases.items()}

    def body(*refs):
        r_in = refs[:n_in]
        refs = refs[n_in + n_after:]
        sems = refs[:n_sem]
        thru = refs[n_sem:n_sem + n_in]
        fresh_refs = refs[n_sem + n_in:n_sem + n_in + len(fresh)]
        token = refs[-1]
        r_out = [thru[by_out[j]] if j in by_out else fresh_refs[fresh.index(j)] for j in range(n_out)]
        rider.start(r_in, r_out, _as_rider_sems(rider, sems))
        token[...] = jnp.zeros_like(token)

    res = pl.pallas_call(
        body, name=name,
        out_shape=_flat_sem_types(rider) + tuple(pltpu.HBM(o.shape, o.dtype) for o in rider.operands)
        + tuple(pltpu.HBM(rider.out_shapes[j].shape, rider.out_shapes[j].dtype) for j in fresh)
        + (jax.ShapeDtypeStruct((8, LANES), F32),),
        in_specs=(HBM,) * n_in + (ANY,) * n_after,
        out_specs=(SEM,) * n_sem + (HBM,) * (n_in + len(fresh)) + (pl.BlockSpec(memory_space=pltpu.VMEM),),
        input_output_aliases={i: n_sem + i for i in range(n_in)}, compiler_params=_IN_FLIGHT,
    )(*[pltpu.with_memory_space_constraint(o, pltpu.HBM) for o in rider.operands], *after)
    return (rider, res[:n_sem], res[n_sem:n_sem + n_in], res[n_sem + n_in:-1]), res[-1]


def _split_continue(handles, after, name, phase):
    rider, sems, thru, fresh_arrays = handles
    n_in, n_out, n_sem = len(rider.operands), len(rider.out_shapes), len(rider.scratch)
    fresh = [j for j in range(n_out) if j not in rider.aliases.values()]
    by_out = {j: i for i, j in rider.aliases.items()}
    n_data = n_in + len(fresh)

    def body(*refs):
        r_in = refs[:n_in]
        fresh_refs = refs[n_in:n_data]
        sem_refs = refs[n_data:n_data + n_sem]
        r_out = [r_in[by_out[j]] if j in by_out else fresh_refs[fresh.index(j)] for j in range(n_out)]
        phase(r_in, r_out, _as_rider_sems(rider, sem_refs))

    data = list(thru) + list(fresh_arrays)
    return pl.pallas_call(
        body, name=name, out_shape=tuple(pltpu.HBM(d.shape, d.dtype) for d in data),
        in_specs=(HBM,) * n_data + (SEM,) * n_sem + (ANY,) * len(after), out_specs=(HBM,) * n_data,
        input_output_aliases={i: i for i in range(n_data)}, compiler_params=_IN_FLIGHT,
    )(*data, *sems, *after)


def _split_middle(handles, after, name):
    rider, sems, thru, _ = handles
    res = _split_continue(handles, after, name, rider.middle)
    return rider, sems, res[:len(thru)], res[len(thru):]


def _split_wait(handles, after, name):
    rider = handles[0]
    n_in, n_out = len(rider.operands), len(rider.out_shapes)
    fresh = [j for j in range(n_out) if j not in rider.aliases.values()]
    by_out = {j: i for i, j in rider.aliases.items()}
    res = _split_continue(handles, after, name, rider.wait)
    return [res[by_out[j]] if j in by_out else res[n_in + fresh.index(j)] for j in range(n_out)], res[:n_in]


def _pair_gather_rider(bufs):
    n = len(bufs)

    def copies(r_out, sems):
        x, y, c = _mesh_pos()
        out = []
        for a in range(n):
            send = functools.partial(
                    pltpu.make_async_remote_copy,
                src_ref=r_out[a].at[c], dst_ref=r_out[a].at[c], send_sem=sems[0].at[a],
                recv_sem=sems[1].at[a], device_id=(x, y, 1 - c), device_id_type=MESH)
            recv = functools.partial(
                    pltpu.make_async_remote_copy,
                src_ref=r_out[a].at[1 - c], dst_ref=r_out[a].at[1 - c], send_sem=sems[0].at[a],
                recv_sem=sems[1].at[a], device_id=(x, y, 1 - c), device_id_type=MESH)
            out.append((send, recv))
        return out

    def start(r_in, r_out, sems):
        for send, _ in copies(r_out, sems):
            send().start()

    def wait(r_in, r_out, sems):
        cps = copies(r_out, sems)
        for _, recv in cps:
            recv().wait_recv()
        for send, _ in cps:
            send().wait_send()

    return _Rider(bufs, [jax.ShapeDtypeStruct(b.shape, b.dtype) for b in bufs], {i: i for i in range(n)},
                  [pltpu.SemaphoreType.DMA((n,)), pltpu.SemaphoreType.DMA((n,))], start, wait)


def _add_own_half(g, recv, pos_arr, name):
    nb, rh, cols = g.shape[0], g.shape[-2], g.shape[-1]

    def body(pos_ref, g_ref, r_ref, send_ref, own_ref):
        s = (g_ref[...] + r_ref[...]).astype(BF16)
        send_ref[...] = s

        @pl.when(pl.program_id(0) == pos_ref[1])
        def _():
            own_ref[...] = s

    blk = pl.BlockSpec((None, rh, cols), lambda j, pos_ref: (j, 0, 0))
    g_spec = blk if g.ndim == 3 else pl.BlockSpec((None, None, rh, cols),
                                                   lambda j, pos_ref: (j, pos_ref[0], 0, 0))
    shape = jax.ShapeDtypeStruct((nb, rh, cols), BF16)
    return pl.pallas_call(
        body, name=name,
        grid_spec=pltpu.PrefetchScalarGridSpec(
            num_scalar_prefetch=1, grid=(nb,), in_specs=[g_spec, blk],
            out_specs=[blk, pl.BlockSpec((None, rh, cols), lambda j, pos_ref: (pos_ref[1], 0, 0))]),
        out_shape=[shape, shape], compiler_params=_params(),
    )(pos_arr, g, recv)


def _sum_chips(gath, pos_arr, name):
    nb, rh, cols = gath.shape

    def body(pos_ref, a_ref, b_ref, c_ref, d_ref, o_ref):
        del pos_ref
        o_ref[...] = ((a_ref[...].astype(F32) + b_ref[...].astype(F32)) + c_ref[...].astype(F32)) \
            + d_ref[...].astype(F32)

    tr = rh // 2 if (rh // 2) % 16 == 0 else rh
    specs = [pl.BlockSpec((None, tr, cols), functools.partial(lambda i, pos_ref, j: (j, i, 0), j=j))
             for j in range(nb)]
    return pl.pallas_call(
        body, name=name,
        grid_spec=pltpu.PrefetchScalarGridSpec(
            num_scalar_prefetch=1, grid=(rh // tr,), in_specs=specs,
            out_specs=pl.BlockSpec((None, tr, cols), lambda i, pos_ref: (pos_ref[0], i, 0))),
        out_shape=jax.ShapeDtypeStruct((2, rh, cols), F32), compiler_params=_params(),
    )(pos_arr, gath, gath, gath, gath)


N_DEVICES = 8


def _small_gather_rider(buf):
    def copies(r_out, sems):
        x, y, c = _mesh_pos()
        me = 4 * x + 2 * y + c
        out = []
        for r in range(1, N_DEVICES):
            px = 1 - x if r & 4 else x
            py = 1 - y if r & 2 else y
            pc = 1 - c if r & 1 else c
            out.append(pltpu.make_async_remote_copy(
                src_ref=r_out[0].at[me], dst_ref=r_out[0].at[me], send_sem=sems[0].at[r - 1],
                recv_sem=sems[1].at[r - 1], device_id=(px, py, pc), device_id_type=MESH))
        return out

    def start(r_in, r_out, sems):
        for cp in copies(r_out, sems):
            cp.start()

    def wait(r_in, r_out, sems):
        cps = copies(r_out, sems)
        for cp in cps:
            cp.wait_recv()
        for cp in cps:
            cp.wait_send()

    return _Rider([buf], [jax.ShapeDtypeStruct(buf.shape, buf.dtype)], {0: 0},
                  [pltpu.SemaphoreType.DMA((N_DEVICES - 1,)), pltpu.SemaphoreType.DMA((N_DEVICES - 1,))],
                  start, wait)


def _sum_devices(buf, name):
    def body(b_ref, o_ref):
        acc = b_ref[0]
        for i in range(1, N_DEVICES):
            acc = acc + b_ref[i]
        o_ref[...] = acc

    return _pallas(body, name=name, grid=(1,), in_specs=[pl.BlockSpec(buf.shape, lambda i: (0, 0, 0))],
                   out_specs=pl.BlockSpec(buf.shape[1:], lambda i: (0, 0)),
                   out_shape=jax.ShapeDtypeStruct(buf.shape[1:], F32), operands=[buf])


def _adamw_math(w, g, m, v):
    m = ADAM_B1 * m + (1.0 - ADAM_B1) * g
    v = ADAM_B2 * v + (1.0 - ADAM_B2) * (g * g)
    m_hat = m / (1.0 - ADAM_B1 ** ADAM_STEP)
    v_hat = v / (1.0 - ADAM_B2 ** ADAM_STEP)
    delta = -ADAM_LR * (m_hat / (jnp.sqrt(v_hat) + ADAM_EPS) + ADAM_WD * w)
    return delta, m, v


def _adamw(w, g, m, v, name, after=()):
    r, c = w.shape
    tr = 128 if r % 128 == 0 else 64
    assert r % tr == 0

    def body(w_ref, g_ref, m_ref, v_ref, go_ref, d_ref, mo_ref, vo_ref):
        gv = g_ref[...]
        d, mn, vn = _adamw_math(w_ref[...], gv, m_ref[...], v_ref[...])
        go_ref[...] = gv
        d_ref[...] = d
        mo_ref[...] = mn
        vo_ref[...] = vn

    blk = pl.BlockSpec((tr, c), lambda i: (i, 0))
    return _pallas(body, name=name, grid=(r // tr,), in_specs=[blk] * 4, out_specs=[blk] * 4,
                   out_shape=[jax.ShapeDtypeStruct((r, c), F32)] * 4, operands=[w, g, m, v], after=after)


def _adamw_small(ws, gs, ms, vs, name):
    n = len(ws)

    def body(*refs):
        w_r, g_r, m_r, v_r = refs[:n], refs[n:2 * n], refs[2 * n:3 * n], refs[3 * n:4 * n]
        d_o, m_o, v_o = refs[4 * n:5 * n], refs[5 * n:6 * n], refs[6 * n:7 * n]
        for i in range(n):
            d, mn, vn = _adamw_math(w_r[i][...], g_r[i][...], m_r[i][...], v_r[i][...])
            d_o[i][...] = d
            m_o[i][...] = mn
            v_o[i][...] = vn

    specs = [pl.BlockSpec(w.shape, lambda i: (0, 0)) for w in ws]
    shapes = [jax.ShapeDtypeStruct(w.shape, F32) for w in ws]
    outs = pl.pallas_call(
        body, name=name, grid=(1,), in_specs=specs * 4, out_specs=specs * 3, out_shape=shapes * 3,
        compiler_params=_params(),
    )(*ws, *gs, *ms, *vs)
    return outs[:n], outs[n:2 * n], outs[2 * n:]


BIG = ("w_in", "w_o_attn", "w_pw_conv", "w_out", "w_ffn_in", "w_ffn_out")
ROW_SHARDED = ("w_out", "w_ffn_out")
SMALL = ("norm1_w", "b_gate", "q_norm_w", "k_norm_w", "conv_w", "conv_b", "conv_ln_w", "conv_ln_b", "norm2_w")
ORDER = ("norm1_w", "w_in", "b_gate", "q_norm_w", "k_norm_w", "w_o_attn", "conv_w", "conv_b", "conv_ln_w",
         "conv_ln_b", "w_pw_conv", "w_out", "norm2_w", "w_ffn_in", "w_ffn_out")
PACK_TILE = 8 * LANES


def _pack_small(parts):
    rows = []
    for p in parts:
        flat = p.reshape(-1)
        pad = (-flat.shape[0]) % PACK_TILE
        rows.append(jnp.pad(flat, (0, pad)).reshape(-1, LANES))
    return jnp.concatenate(rows, axis=0)


def _unpack_small(packed, shapes):
    out, row = [], 0
    for shp in shapes:
        size = int(np.prod(shp))
        nrow = -(-size // PACK_TILE) * (PACK_TILE // LANES)
        out.append(packed[row:row + nrow].reshape(-1)[:size].reshape(shp))
        row += nrow
    return out


def kernel(x, positions, norm1_w, w_in, b_gate, q_norm_w, k_norm_w, w_o_attn, conv_w, conv_b, conv_ln_w, conv_ln_b, w_pw_conv, w_out, norm2_w, w_ffn_in, w_ffn_out, loss_target, m_norm1_w, m_w_in, m_b_gate, m_q_norm_w, m_k_norm_w, m_w_o_attn, m_conv_w, m_conv_b, m_conv_ln_w, m_conv_ln_b, m_w_pw_conv, m_w_out, m_norm2_w, m_w_ffn_in, m_w_ffn_out, v_norm1_w, v_w_in, v_b_gate, v_q_norm_w, v_k_norm_w, v_w_o_attn, v_conv_w, v_conv_b, v_conv_ln_w, v_conv_ln_b, v_w_pw_conv, v_w_out, v_norm2_w, v_w_ffn_in, v_w_ffn_out):
    w = dict(norm1_w=norm1_w, w_in=w_in, b_gate=b_gate, q_norm_w=q_norm_w, k_norm_w=k_norm_w, w_o_attn=w_o_attn,
             conv_w=conv_w, conv_b=conv_b, conv_ln_w=conv_ln_w, conv_ln_b=conv_ln_b, w_pw_conv=w_pw_conv,
             w_out=w_out, norm2_w=norm2_w, w_ffn_in=w_ffn_in, w_ffn_out=w_ffn_out)
    m = dict(norm1_w=m_norm1_w, w_in=m_w_in, b_gate=m_b_gate, q_norm_w=m_q_norm_w, k_norm_w=m_k_norm_w,
             w_o_attn=m_w_o_attn, conv_w=m_conv_w, conv_b=m_conv_b, conv_ln_w=m_conv_ln_w,
             conv_ln_b=m_conv_ln_b, w_pw_conv=m_w_pw_conv, w_out=m_w_out, norm2_w=m_norm2_w,
             w_ffn_in=m_w_ffn_in, w_ffn_out=m_w_ffn_out)
    v = dict(norm1_w=v_norm1_w, w_in=v_w_in, b_gate=v_b_gate, q_norm_w=v_q_norm_w, k_norm_w=v_k_norm_w,
             w_o_attn=v_w_o_attn, conv_w=v_conv_w, conv_b=v_conv_b, conv_ln_w=v_conv_ln_w,
             conv_ln_b=v_conv_ln_b, w_pw_conv=v_w_pw_conv, w_out=v_w_out, norm2_w=v_norm2_w,
             w_ffn_in=v_w_ffn_in, w_ffn_out=v_w_ffn_out)
    cx, cy, cc = _mesh_pos()
    chip = 2 * cx + cy

    chip_arr = chip.reshape(1).astype(jnp.int32)
    pos_arr = jnp.stack([cc, chip]).astype(jnp.int32)
    halves = lambda buf: buf.reshape(N_CHIPS, 2, buf.shape[1] // 2, buf.shape[2])
    w_in_buf = halves(_cast_into_slot(w["w_in"][0], chip_arr, BF16, "cast_w_in"))
    small_bufs = [_cast_into_slot(w[n][0], chip_arr, F32, f"slot_{n}") for n in ("conv_w", "b_gate")]
    first_gather, started = _split_start(_gather_both_legs_rider([w_in_buf], small_bufs, NEAR_CHIPS),
                                         "allgather_w_in_near_start")
    late_bufs = [halves(_cast_into_slot(w[n][0], chip_arr, BF16, f"cast_{n}", after=[started]))
                 for n in LATE_GATHER]
    wts = dict(norm1_w=norm1_w, q_norm_w=q_norm_w, k_norm_w=k_norm_w, conv_b=conv_b, conv_ln_w=conv_ln_w,
               conv_ln_b=conv_ln_b, norm2_w=norm2_w)

    loss, grad_x, g, reduced, w_in_in_flight = _forward_backward(
        x[0], positions.reshape(-1, 1), loss_target[0], wts, first_gather, late_bufs, pos_arr)
    grads = {n: b.reshape(-1, b.shape[2]) for n, b in reduced.items()}

    w_in_in_flight, started = w_in_in_flight
    delta, new_m, new_v = {}, {}, {}
    for n in EARLY_REDUCE:
        grads[n], delta[n], new_m[n], new_v[n] = _adamw(w[n][0], grads[n], m[n][0], v[n][0], f"adamw_{n}",
                                                        after=[started])
    small_parts = [loss] + [g[n] for n in SMALL]
    small_shapes = [p.shape for p in small_parts]
    device_arr = (4 * cx + 2 * cy + cc).reshape(1).astype(jnp.int32)
    small_buf = _cast_into_slot(_pack_small(small_parts), device_arr, F32, "slot_small", n_slots=N_DEVICES)

    (by_chip_w_in,), _ = _split_wait(w_in_in_flight, after=[delta[n] for n in EARLY_REDUCE] + [small_buf],
                                     name="grads_w_in_exchange_wait")
    half_w_in = _sum_chips(by_chip_w_in, pos_arr, "grads_chip_sum_w_in")
    shard_w_in, small_buf = _comm_call(
        _riders_together(_pair_gather_rider([half_w_in]), _small_gather_rider(small_buf)),
        "grads_pair_gather_w_in_small_gather")
    summed = _sum_devices(small_buf, "small_sum")
    reduced = _unpack_small(summed, small_shapes)
    loss_total = reduced[0].reshape(())
    for n, r in zip(SMALL, reduced[1:]):
        grads[n] = r
    ch_shard = conv_w.shape[2]
    grads["conv_w"] = lax.dynamic_slice_in_dim(grads["conv_w"], chip * ch_shard, ch_shard, axis=1)
    d_shard = b_gate.shape[2]
    grads["b_gate"] = lax.dynamic_slice_in_dim(grads["b_gate"], chip * d_shard, d_shard, axis=1)

    grads["w_in"], delta["w_in"], new_m["w_in"], new_v["w_in"] = _adamw(
        w["w_in"][0], shard_w_in.reshape(-1, shard_w_in.shape[2]), m["w_in"][0], v["w_in"][0], "adamw_w_in")
    flat2 = lambda a: a.reshape(-1, a.shape[-1])
    d_s, m_s, v_s = _adamw_small([flat2(w[n]) for n in SMALL], [flat2(grads[n]) for n in SMALL],
                                 [flat2(m[n]) for n in SMALL], [flat2(v[n]) for n in SMALL], "adamw_small")
    for i, n in enumerate(SMALL):
        delta[n], new_m[n], new_v[n] = d_s[i], m_s[i], v_s[i]

    shaped = lambda d, n: d[n].reshape(w[n].shape)
    return (loss_total, grad_x[None], *[shaped(grads, n) for n in ORDER], *[shaped(delta, n) for n in ORDER],
            *[shaped(new_m, n) for n in ORDER], *[shaped(new_v, n) for n in ORDER])
```

```python
import functools

import numpy as np
import jax
import jax.numpy as jnp
from jax import lax
from jax.experimental import pallas as pl
from jax.experimental.pallas import tpu as pltpu

F32 = jnp.float32
BF16 = jnp.bfloat16
MESH = pl.DeviceIdType.MESH
ANY = pl.BlockSpec(memory_space=pl.ANY)

HEAD_DIM = 64
N_SLOT_HEADS = 8
DILATIONS = (1, 4, 16)
HALF_SPAN = 64
ROPE_THETA = 500000.0
ROT_DIM = 16
CONV_WIDTH = 31
EPS = 1e-6
NEG_INF = -1e30
ADAM_LR, ADAM_B1, ADAM_B2, ADAM_EPS, ADAM_WD, ADAM_STEP = 0.001, 0.9, 0.999, 1e-08, 0.01, 10

LANES = 128
QBLK = 128
KWIN = QBLK + 2 * HALF_SPAN
VMEM_LIMIT = 48 * 1024 * 1024
N_CHIPS = 4


def _params(**kw):
    return pltpu.CompilerParams(vmem_limit_bytes=VMEM_LIMIT, **kw)


class _Rider:
    def __init__(self, operands, out_shapes, aliases, scratch, start, wait, middle=None):
        self.operands, self.out_shapes, self.aliases = list(operands), list(out_shapes), dict(aliases)
        self.scratch, self.start, self.wait = list(scratch), start, wait
        self.middle = middle


def _riders_together(a, b):
    n_in, n_out, n_sc = len(a.operands), len(a.out_shapes), len(a.scratch)
    aliases = dict(a.aliases)
    aliases.update({n_in + src: n_out + dst for src, dst in b.aliases.items()})

    def start(r_in, r_out, r_sc):
        a.start(r_in[:n_in], r_out[:n_out], r_sc[:n_sc])
        b.start(r_in[n_in:], r_out[n_out:], r_sc[n_sc:])

    def wait(r_in, r_out, r_sc):
        a.wait(r_in[:n_in], r_out[:n_out], r_sc[:n_sc])
        b.wait(r_in[n_in:], r_out[n_out:], r_sc[n_sc:])

    return _Rider(a.operands + b.operands, a.out_shapes + b.out_shapes, aliases, a.scratch + b.scratch, start, wait)


def _pallas(body, *, name, grid, in_specs, out_specs, out_shape, operands, scratch_shapes=(), aliases=None,
            rider=None, after=()):
    single = not isinstance(out_specs, (list, tuple))
    out_specs_l = [out_specs] if single else list(out_specs)
    out_shape_l = [out_shape] if single else list(out_shape)
    aliases = dict(aliases or {})

    def call(fn, all_in_specs, all_out_specs, all_out_shape, all_scratch, all_aliases, all_operands):
        return pl.pallas_call(
            fn, name=name, grid=grid, in_specs=all_in_specs, out_specs=all_out_specs, out_shape=all_out_shape,
            scratch_shapes=all_scratch, input_output_aliases=all_aliases, compiler_params=_params(),
        )(*all_operands)

    if rider is None:
        n_main = len(in_specs)

        def ordered(*refs):
            body(*refs[:n_main], *refs[n_main + len(after):])

        res = call(ordered if after else body, list(in_specs) + [ANY] * len(after), out_specs_l, out_shape_l,
                   list(scratch_shapes), aliases, list(operands) + list(after))
        return res[0] if single else res
    assert not after
    n_in, n_rin = len(in_specs), len(rider.operands)
    n_out, n_rout = len(out_specs_l), len(rider.out_shapes)
    n_sc = len(scratch_shapes)

    def wrapped(*refs):
        main_in, r_in = refs[:n_in], refs[n_in:n_in + n_rin]
        o0 = n_in + n_rin
        main_out, r_out = refs[o0:o0 + n_out], refs[o0 + n_out:o0 + n_out + n_rout]
        s0 = o0 + n_out + n_rout
        main_sc, r_sc = refs[s0:s0 + n_sc], refs[s0 + n_sc:]
        ids = [pl.program_id(d) for d in range(len(grid))]
        first = functools.reduce(jnp.logical_and, [i == 0 for i in ids])
        last = functools.reduce(jnp.logical_and, [i == n - 1 for i, n in zip(ids, grid)])

        @pl.when(first)
        def _():
            rider.start(r_in, r_out, r_sc)

        body(*main_in, *main_out, *main_sc)

        @pl.when(last)
        def _():
            rider.wait(r_in, r_out, r_sc)

    for src, dst in rider.aliases.items():
        aliases[n_in + src] = n_out + dst
    res = call(wrapped, list(in_specs) + [ANY] * n_rin, out_specs_l + [ANY] * n_rout,
               out_shape_l + rider.out_shapes, list(scratch_shapes) + rider.scratch, aliases,
               list(operands) + rider.operands)
    main = res[:n_out]
    return (main[0] if single else main), res[n_out:]


def _matmul(a, b, *, mode, tm, tn, tk, out_dtype, name, b_blocked=False,
            out_blocked=None, cols_outer=False, rider=None, after=()):
    a_shape = a.shape
    if mode == "nn":
        m_dim, k_dim = a_shape
        n_dim = b.shape[0] * b.shape[2] if b_blocked else b.shape[1]
        rows, cols, red = m_dim, n_dim, k_dim
    elif mode == "nt":
        m_dim, n_dim = a_shape
        k_dim = b.shape[1] if b_blocked else b.shape[0]
        rows, cols, red = m_dim, k_dim, n_dim
    else:
        m_dim, k_dim = a_shape
        n_dim = b.shape[1]
        rows, cols, red = k_dim, n_dim, m_dim
    assert rows % tm == 0 and cols % tn == 0 and red % tk == 0, (name, rows, cols, red)
    ni, nj, nk = rows // tm, cols // tn, red // tk

    if mode == "nn":
        a_spec = pl.BlockSpec((tm, tk), lambda i, j, k: (i, k))
        if b_blocked:
            per = b.shape[2] // tn
            b_spec = pl.BlockSpec((None, tk, tn), lambda i, j, k: (j // per, k, j % per))
        else:
            b_spec = pl.BlockSpec((tk, tn), lambda i, j, k: (k, j))
        dims = (((1,), (0,)), ((), ()))
    elif mode == "nt":
        a_spec = pl.BlockSpec((tm, tk), lambda i, j, k: (i, k))
        if b_blocked:
            per = b.shape[2] // tk
            b_spec = pl.BlockSpec((None, tn, tk), lambda i, j, k: (k // per, j, k % per))
        else:
            b_spec = pl.BlockSpec((tn, tk), lambda i, j, k: (j, k))
        dims = (((1,), (1,)), ((), ()))
    else:
        a_spec = pl.BlockSpec((tk, tm), lambda i, j, k: (k, i))
        b_spec = pl.BlockSpec((tk, tn), lambda i, j, k: (k, j))
        dims = (((0,), (0,)), ((), ()))

    if out_blocked:
        per_o = (cols // out_blocked) // tn
        out_spec = pl.BlockSpec((None, tm, tn), lambda i, j, k: (j // per_o, i, j % per_o))
        out_shape = jax.ShapeDtypeStruct((out_blocked, rows, cols // out_blocked), out_dtype)
    else:
        out_spec = pl.BlockSpec((tm, tn), lambda i, j, k: (i, j))
        out_shape = jax.ShapeDtypeStruct((rows, cols), out_dtype)

    def body(a_ref, b_ref, o_ref, *acc):
        prod = lax.dot_general(a_ref[...], b_ref[...], dims, preferred_element_type=F32)
        if nk == 1:
            o_ref[...] = prod.astype(out_dtype)
        else:
            acc_ref, = acc
            k = pl.program_id(2)

            @pl.when(k == 0)
            def _():
                acc_ref[...] = prod

            @pl.when(k > 0)
            def _():
                acc_ref[...] += prod

            @pl.when(k == nk - 1)
            def _():
                o_ref[...] = acc_ref[...].astype(out_dtype)

    scratch = [pltpu.VMEM((tm, tn), F32)] if nk > 1 else []
    grid = (ni, nj, nk)
    if cols_outer:
        swap = lambda spec: pl.BlockSpec(spec.block_shape, lambda j, i, k, f=spec.index_map: f(i, j, k))
        a_spec, b_spec, out_spec, grid = swap(a_spec), swap(b_spec), swap(out_spec), (nj, ni, nk)
    return _pallas(body, name=name, grid=grid, in_specs=[a_spec, b_spec], out_specs=out_spec,
                   out_shape=out_shape, operands=[a, b], scratch_shapes=scratch, rider=rider, after=after)


def _proj_by_slot(h, w_in, order, first, count, name, proj=None, after=()):
    s, k = h.shape
    nb = w_in.shape[2]
    tm = 512
    prev = [] if proj is None else [proj]

    def body(order_ref, h_ref, w_ref, *refs):
        refs[-1][...] = jnp.dot(h_ref[...], w_ref[...], preferred_element_type=F32)

    return pl.pallas_call(
        body, name=name,
        grid_spec=pltpu.PrefetchScalarGridSpec(
            num_scalar_prefetch=1, grid=(count, s // tm),
            in_specs=[pl.BlockSpec((tm, k), lambda j, i, order_ref: (i, 0)),
                      pl.BlockSpec((None, k, nb), lambda j, i, order_ref: (order_ref[first + j], 0, 0))]
            + [ANY] * (len(prev) + len(after)),
            out_specs=pl.BlockSpec((tm, nb), lambda j, i, order_ref: (i, order_ref[first + j]))),
        out_shape=jax.ShapeDtypeStruct((s, N_CHIPS * nb), F32),
        input_output_aliases={3: 0} if prev else {}, compiler_params=_params(),
    )(order, h, w_in, *prev, *after)


def _rmsnorm_fwd(x, w, name):
    s, d = x.shape
    tm = 256

    def body(x_ref, w_ref, o_ref):
        xv = x_ref[...]
        rstd = lax.rsqrt(jnp.mean(xv * xv, axis=-1, keepdims=True) + EPS)
        o_ref[...] = (xv * rstd * w_ref[...]).astype(BF16)

    return pl.pallas_call(
        body, name=name, grid=(s // tm,),
        in_specs=[pl.BlockSpec((tm, d), lambda i: (i, 0)), pl.BlockSpec((1, d), lambda i: (0, 0))],
        out_specs=pl.BlockSpec((tm, d), lambda i: (i, 0)),
        out_shape=jax.ShapeDtypeStruct((s, d), BF16), compiler_params=_params(),
    )(x, w)


def _matmul_nt_rmsnorm_bwd(dz, w_blocked, x, w, dres, name, rider=None, after=()):
    s, d = x.shape
    nk, _, nb = w_blocked.shape
    tm = 512
    nt_dims = (((1,), (1,)), ((), ()))

    def body(a_ref, b_ref, x_ref, w_ref, dres_ref, dx_ref, dxb_ref, dw_ref, acc_ref):
        i, k = pl.program_id(0), pl.program_id(1)
        prod = lax.dot_general(a_ref[...], b_ref[...], nt_dims, preferred_element_type=F32)

        @pl.when(k == 0)
        def _():
            acc_ref[...] = prod

        @pl.when(k > 0)
        def _():
            acc_ref[...] += prod

        @pl.when(k == nk - 1)
        def _():
            xv = x_ref[...]
            rstd = lax.rsqrt(jnp.mean(xv * xv, axis=-1, keepdims=True) + EPS)
            xhat = xv * rstd
            dhv = acc_ref[...]
            g = dhv * w_ref[...]
            dx = rstd * (g - xhat * jnp.mean(g * xhat, axis=-1, keepdims=True)) + dres_ref[...]
            dx_ref[...] = dx
            dxb_ref[...] = dx.astype(BF16)
            part = jnp.sum(dhv * xhat, axis=0, keepdims=True)

            @pl.when(i == 0)
            def _():
                dw_ref[...] = part

            @pl.when(i > 0)
            def _():
                dw_ref[...] += part

    row = pl.BlockSpec((tm, d), lambda i, k: (i, 0))
    vec = pl.BlockSpec((1, d), lambda i, k: (0, 0))
    return _pallas(
        body, name=name, grid=(s // tm, nk),
        in_specs=[pl.BlockSpec((tm, nb), lambda i, k: (i, k)), pl.BlockSpec((None, d, nb), lambda i, k: (k, 0, 0)),
                  row, vec, row],
        out_specs=[row, row, vec],
        out_shape=[jax.ShapeDtypeStruct((s, d), F32), jax.ShapeDtypeStruct((s, d), BF16),
                   jax.ShapeDtypeStruct((1, d), F32)],
        operands=[dz, w_blocked, x, w, dres], scratch_shapes=[pltpu.VMEM((tm, d), F32)], rider=rider, after=after)


def _rope_consts():
    lane = np.arange(LANES)
    in_head = lane % HEAD_DIM
    inv_freq = ROPE_THETA ** (-jnp.arange(0, ROT_DIM, 2, dtype=F32) / ROT_DIM)
    invf = jnp.where(jnp.asarray(in_head < ROT_DIM), jnp.tile(inv_freq, LANES // (ROT_DIM // 2)), 0.0)
    m_a = np.where(in_head < ROT_DIM // 2, -1.0, 0.0).astype(np.float32)
    m_b = np.where((in_head >= ROT_DIM // 2) & (in_head < ROT_DIM), 1.0, 0.0).astype(np.float32)
    block_diag = (lane[:, None] // HEAD_DIM == lane[None, :] // HEAD_DIM).astype(np.float32)
    return (invf.reshape(1, LANES).astype(F32), jnp.asarray(m_a).reshape(1, LANES),
            jnp.asarray(m_b).reshape(1, LANES), jnp.asarray(block_diag, dtype=BF16))


def _head_sums(v, bd):
    hi = v.astype(BF16)
    lo = (v - hi.astype(F32)).astype(BF16)
    return jnp.dot(hi, bd, preferred_element_type=F32) + jnp.dot(lo, bd, preferred_element_type=F32)


def _qk_fwd(proj, pos_col, qw2, kw2, consts, name, rider=None):
    s = proj.shape[0]
    width = 3 * N_SLOT_HEADS * HEAD_DIM
    tm = 128
    invf, m_a, m_b, bd = consts
    scale = HEAD_DIM ** -0.5

    def body(q_ref, k_ref, pos_ref, qw_ref, kw_ref, invf_ref, ma_ref, mb_ref, bd_ref, qo_ref, ko_ref):
        ang = pos_ref[...].astype(F32) * invf_ref[...]
        cos = jnp.cos(ang)
        sin = jnp.sin(ang)
        s_a = sin * ma_ref[...]
        s_b = sin * mb_ref[...]
        bdv = bd_ref[...]
        for src, w_ref, dst, sc in ((q_ref, qw_ref, qo_ref, scale), (k_ref, kw_ref, ko_ref, 1.0)):
            for cb in range(width // LANES):
                cols = slice(cb * LANES, (cb + 1) * LANES)
                t = src[:, cols]
                rstd = lax.rsqrt(_head_sums(t * t, bdv) * (1.0 / HEAD_DIM) + EPS)
                y = t * rstd * w_ref[...]
                r = y * cos + pltpu.roll(y, LANES - 8, axis=1) * s_a + pltpu.roll(y, 8, axis=1) * s_b
                dst[:, cols] = r * sc if sc != 1.0 else r

    vec = pl.BlockSpec((1, LANES), lambda i: (0, 0))
    return _pallas(
        body, name=name, grid=(s // tm,),
        in_specs=[pl.BlockSpec((tm, width), lambda i: (i, 0)), pl.BlockSpec((tm, width), lambda i: (i, 1)),
                  pl.BlockSpec((tm, 1), lambda i: (i, 0)), vec, vec, vec, vec, vec,
                  pl.BlockSpec((LANES, LANES), lambda i: (0, 0))],
        out_specs=[pl.BlockSpec((tm, width), lambda i: (i, 0))] * 2,
        out_shape=[jax.ShapeDtypeStruct((s, width), F32)] * 2,
        operands=[proj, proj, pos_col, qw2, kw2, invf, m_a, m_b, bd], rider=rider)


def _qk_bwd(dqn, dkn, dv, da, db, dgl, proj, pos_col, qw2, kw2, consts, name, rider=None):
    s = proj.shape[0]
    width = 3 * N_SLOT_HEADS * HEAD_DIM
    ch = da.shape[1]
    gate_w = dgl.shape[1]
    out_w = 3 * width + 2 * ch + gate_w
    assert out_w == proj.shape[1]
    tm = 128
    invf, m_a, m_b, bd = consts
    scale = HEAD_DIM ** -0.5

    def body(dq_ref, dk_ref, dv_ref, da_ref, db_ref, dgl_ref, q_ref, k_ref, pos_ref, qw_ref, kw_ref,
             invf_ref, ma_ref, mb_ref, bd_ref, out_ref, dqw_ref, dkw_ref):
        ang = pos_ref[...].astype(F32) * invf_ref[...]
        cos = jnp.cos(ang)
        sin = jnp.sin(ang)
        s_a = sin * ma_ref[...]
        s_b = sin * mb_ref[...]
        bdv = bd_ref[...]
        first = pl.program_id(0) == 0
        for src, dsrc, w_ref, col0, dw_ref, sc in ((q_ref, dq_ref, qw_ref, 0, dqw_ref, scale),
                                                   (k_ref, dk_ref, kw_ref, width, dkw_ref, 1.0)):
            dw_acc = jnp.zeros((1, LANES), F32)
            for cb in range(width // LANES):
                cols = slice(cb * LANES, (cb + 1) * LANES)
                t = src[:, cols]
                dr = dsrc[:, cols]
                if sc != 1.0:
                    dr = dr * sc
                dy = dr * cos + pltpu.roll(dr * s_a, 8, axis=1) + pltpu.roll(dr * s_b, LANES - 8, axis=1)
                rstd = lax.rsqrt(_head_sums(t * t, bdv) * (1.0 / HEAD_DIM) + EPS)
                xhat = t * rstd
                g = dy * w_ref[...]
                dt = rstd * (g - xhat * (_head_sums(g * xhat, bdv) * (1.0 / HEAD_DIM)))
                out_ref[:, col0 + cb * LANES: col0 + (cb + 1) * LANES] = dt.astype(BF16)
                dw_acc = dw_acc + jnp.sum(dy * xhat, axis=0, keepdims=True)
            dw_acc = dw_acc + pltpu.roll(dw_acc, HEAD_DIM, axis=1)

            @pl.when(first)
            def _(dw_ref=dw_ref, dw_acc=dw_acc):
                dw_ref[...] = dw_acc

            @pl.when(jnp.logical_not(first))
            def _(dw_ref=dw_ref, dw_acc=dw_acc):
                dw_ref[...] += dw_acc
        out_ref[:, 2 * width: 3 * width] = dv_ref[...].astype(BF16)
        out_ref[:, 3 * width: 3 * width + ch] = da_ref[...]
        out_ref[:, 3 * width + ch: 3 * width + 2 * ch] = db_ref[...]
        out_ref[:, 3 * width + 2 * ch: out_w] = dgl_ref[...]

    vec = pl.BlockSpec((1, LANES), lambda i: (0, 0))
    blk = lambda c: pl.BlockSpec((tm, width), lambda i: (i, c))
    cblk = pl.BlockSpec((tm, ch), lambda i: (i, 0))
    return _pallas(
        body, name=name, grid=(s // tm,),
        in_specs=[blk(0), blk(0), blk(0), cblk, cblk, pl.BlockSpec((tm, gate_w), lambda i: (i, 0)),
                  blk(0), blk(1), pl.BlockSpec((tm, 1), lambda i: (i, 0)), vec, vec, vec, vec, vec,
                  pl.BlockSpec((LANES, LANES), lambda i: (0, 0))],
        out_specs=[pl.BlockSpec((tm, out_w), lambda i: (i, 0)), vec, vec],
        out_shape=[jax.ShapeDtypeStruct((s, out_w), BF16)] + [jax.ShapeDtypeStruct((1, LANES), F32)] * 2,
        operands=[dqn, dkn, dv, da, db, dgl, proj, proj, pos_col, qw2, kw2, invf, m_a, m_b, bd],
        rider=rider)


def _row_chunks(n_rows, fn, chunk=256):
    def step(i, c):
        fn(pl.ds(pl.multiple_of(i * chunk, chunk), chunk))
        return c
    lax.fori_loop(0, n_rows // chunk, step, 0)


def _to_residue_major(dst, src, s, d, dst_off=0, cast=None):
    seq = s // d
    for r in range(d):
        v = src[...] if d == 1 else src[pl.ds(r, seq, stride=d), :]
        dst[dst_off + r * seq: dst_off + (r + 1) * seq, :] = v if cast is None else v.astype(cast)


def _from_residue_major(dst, src, s, d, src_off=0):
    seq = s // d
    for r in range(d):
        v = src[src_off + r * seq: src_off + (r + 1) * seq, :]
        if d == 1:
            dst[...] = v
        else:
            dst[pl.ds(r, seq, stride=d), :] = v


def _band_bias():
    qi = lax.broadcasted_iota(jnp.int32, (QBLK, KWIN), 0)
    kj = lax.broadcasted_iota(jnp.int32, (QBLK, KWIN), 1)
    return jnp.where(jnp.abs(kj - HALF_SPAN - qi) <= HALF_SPAN, 0.0, NEG_INF).astype(F32)


def _range_bias(base, seq):
    kj = lax.broadcasted_iota(jnp.int32, (1, KWIN), 1)
    lo = (base & -seq) - base + HALF_SPAN
    return jnp.where((kj >= lo) & (kj < lo + seq), 0.0, NEG_INF).astype(F32)


def _skewed_blocks(n_blk, produce, consume):
    produce(0, 0)
    for b in range(n_blk):
        consume(b, b % 2)
        if b + 1 < n_blk:
            produce(b + 1, (b + 1) % 2)


def _block_base(b):
    return b * QBLK if isinstance(b, int) else pl.multiple_of(b * QBLK, QBLK)


def _attn_fwd(qn, kn, proj, name, rider=None):
    s = qn.shape[0]
    n_pairs = N_SLOT_HEADS * HEAD_DIM // LANES
    v_col0 = 2 * qn.shape[1] // LANES
    nt_dims = (((1,), (1,)), ((), ()))

    def body(q_ref, k_ref, v_ref, attn_ref, lse_ref, attn_b_ref, q_rm, k_rm, v_rm, acc_rm, m_rm, l_rm,
             acc_p, m_p, l_p, m_run, l_run, acc_run, band, s_buf, m_buf):
        g = pl.program_id(1)
        zpad = jnp.zeros((HALF_SPAN, LANES), BF16)
        k_rm[0:HALF_SPAN, :] = zpad
        k_rm[s + HALF_SPAN: s + 2 * HALF_SPAN, :] = zpad
        v_rm[0:HALF_SPAN, 0:LANES] = zpad
        v_rm[s + HALF_SPAN: s + 2 * HALF_SPAN, 0:LANES] = zpad

        def ones_rows(rows):
            v_rm[pl.ds(rows.start, rows.size), LANES:2 * LANES] = jnp.ones((rows.size, LANES), BF16)

        _row_chunks(s + 2 * HALF_SPAN, ones_rows, chunk=2 * HALF_SPAN)
        band[...] = _band_bias()
        lane = lax.broadcasted_iota(jnp.int32, (QBLK, LANES), 1)
        low = lane < HEAD_DIM
        n_blk = s // QBLK

        for gi, d in enumerate(DILATIONS):
            @pl.when(g == gi)
            def _(gi=gi, d=d):
                seq = s // d
                _to_residue_major(q_rm, q_ref, s, d, cast=BF16)
                _to_residue_major(k_rm, k_ref, s, d, dst_off=HALF_SPAN, cast=BF16)
                _to_residue_major(v_rm.at[:, 0:LANES], v_ref, s, d, dst_off=HALF_SPAN, cast=BF16)

                def scores(b, slot):
                    base = _block_base(b)
                    q = q_rm[pl.ds(base, QBLK), :]
                    zero = jnp.zeros_like(q)
                    q2 = jnp.concatenate([jnp.where(low, q, zero), jnp.where(low, zero, q)], axis=0)
                    sc = lax.dot_general(q2, k_rm[pl.ds(base, KWIN), :], nt_dims, preferred_element_type=F32)
                    bias = band[...] + _range_bias(base, seq)
                    for hh in range(2):
                        rows = slice(hh * QBLK, (hh + 1) * QBLK)
                        sh = sc[rows, :] + bias
                        s_buf[slot, rows, :] = sh
                        m_buf[slot, rows, :] = jnp.broadcast_to(jnp.max(sh, axis=-1, keepdims=True), (QBLK, LANES))

                def outputs(b, slot):
                    base = _block_base(b)
                    sv = s_buf[slot]
                    mb = m_buf[slot]
                    p = jnp.exp(jnp.concatenate([sv[:, 0:LANES] - mb, sv[:, LANES:2 * LANES] - mb], axis=1))
                    pv = jnp.dot(p.astype(BF16), v_rm[pl.ds(base, KWIN), :], preferred_element_type=F32)
                    rows = pl.ds(base, QBLK)
                    acc_rm[rows, :] = jnp.where(low, pv[0:QBLK, 0:LANES], pv[QBLK:2 * QBLK, 0:LANES])
                    l_rm[rows, :] = jnp.where(low, pv[0:QBLK, LANES:2 * LANES], pv[QBLK:2 * QBLK, LANES:2 * LANES])
                    m_rm[rows, :] = jnp.where(low, mb[0:QBLK, :], mb[QBLK:2 * QBLK, :])

                _skewed_blocks(n_blk, scores, outputs)
                if d == 1:
                    src = (acc_rm, m_rm, l_rm)
                else:
                    for dst_, src_ in ((acc_p, acc_rm), (m_p, m_rm), (l_p, l_rm)):
                        _from_residue_major(dst_, src_, s, d)
                    src = (acc_p, m_p, l_p)

                def combine(rows):
                    a_g, m_g, l_g = src[0][rows, :], src[1][rows, :], src[2][rows, :]
                    if gi == 0:
                        m_new, l_new, a_new = m_g, l_g, a_g
                    else:
                        m_old = m_run[rows, :]
                        m_new = jnp.maximum(m_old, m_g)
                        w_old = jnp.exp(m_old - m_new)
                        w_g = jnp.exp(m_g - m_new)
                        l_new = l_run[rows, :] * w_old + l_g * w_g
                        a_new = acc_run[rows, :] * w_old + a_g * w_g
                    if gi == len(DILATIONS) - 1:
                        out = a_new / l_new
                        attn_ref[rows, :] = out
                        attn_b_ref[rows, :] = out.astype(BF16)
                        lse_ref[rows, :] = m_new + jnp.log(l_new)
                    else:
                        m_run[rows, :] = m_new
                        l_run[rows, :] = l_new
                        acc_run[rows, :] = a_new

                _row_chunks(s, combine)

    qk_spec = pl.BlockSpec((s, LANES), lambda hp, g: (0, g * n_pairs + hp))
    v_spec = pl.BlockSpec((s, LANES), lambda hp, g: (0, v_col0 + g * n_pairs + hp))
    o_spec = pl.BlockSpec((s, LANES), lambda hp, g: (0, hp))
    f32buf = pltpu.VMEM((s, LANES), F32)
    return _pallas(
        body, name=name, grid=(n_pairs, len(DILATIONS)), in_specs=[qk_spec, qk_spec, v_spec],
        out_specs=[o_spec, o_spec, o_spec],
        out_shape=[jax.ShapeDtypeStruct((s, n_pairs * LANES), F32)] * 2
        + [jax.ShapeDtypeStruct((s, n_pairs * LANES), BF16)],
        operands=[qn, kn, proj],
        scratch_shapes=[pltpu.VMEM((s, LANES), BF16), pltpu.VMEM((s + 2 * HALF_SPAN, LANES), BF16),
                        pltpu.VMEM((s + 2 * HALF_SPAN, 2 * LANES), BF16)] + [f32buf] * 9
        + [pltpu.VMEM((QBLK, KWIN), F32), pltpu.VMEM((2, 2 * QBLK, KWIN), F32),
           pltpu.VMEM((2, 2 * QBLK, LANES), F32)],
        rider=rider)


def _attn_bwd(qn, kn, proj, dattn, attn, lse, bd, name, rider=None):
    s = qn.shape[0]
    n_pairs = N_SLOT_HEADS * HEAD_DIM // LANES
    v_col0 = 2 * qn.shape[1] // LANES
    nt_dims = (((1,), (1,)), ((), ()))
    tn_dims = (((0,), (0,)), ((), ()))
    spad = s + 2 * HALF_SPAN

    def body(q_ref, k_ref, v_ref, do_ref, o_ref, lse_ref, bd_ref, dq_ref, dk_ref, dv_ref,
             q_rm, k_rm, v_rm, do_rm, lse0_rm, lse1_rm, dd0_rm, dd1_rm, dq_rm, dk_rm, dv_rm,
             lse0_p, lse1_p, dd0_p, dd1_p, band, p_buf, ds_buf):
        g = pl.program_id(1)
        zpad = jnp.zeros((HALF_SPAN, LANES), BF16)
        for buf in (k_rm, v_rm):
            buf[0:HALF_SPAN, :] = zpad
            buf[s + HALF_SPAN: spad, :] = zpad
        zf = jnp.zeros((HALF_SPAN, LANES), F32)
        for buf in (dk_rm, dv_rm):
            buf[0:HALF_SPAN, :] = zf
            buf[s + HALF_SPAN: spad, :] = zf
        band[...] = _band_bias()

        def clear(rows):
            z = jnp.zeros((rows.size, LANES), F32)
            dk_rm[pl.ds(rows.start + HALF_SPAN, rows.size), :] = z
            dv_rm[pl.ds(rows.start + HALF_SPAN, rows.size), :] = z

        _row_chunks(s, clear)

        def prepare(rows):
            lo = lax.broadcasted_iota(jnp.int32, (rows.size, LANES), 1) < HEAD_DIM
            dsum = _head_sums(do_ref[rows, :] * o_ref[rows, :], bd_ref[...])
            dswap = pltpu.roll(dsum, HEAD_DIM, axis=1)
            dd0_p[rows, :] = jnp.where(lo, dsum, dswap)
            dd1_p[rows, :] = jnp.where(lo, dswap, dsum)
            lv = lse_ref[rows, :]
            lswap = pltpu.roll(lv, HEAD_DIM, axis=1)
            lse0_p[rows, :] = jnp.where(lo, lv, lswap)
            lse1_p[rows, :] = jnp.where(lo, lswap, lv)

        @pl.when(g == 0)
        def _():
            _row_chunks(s, prepare)
        lane = lax.broadcasted_iota(jnp.int32, (QBLK, LANES), 1)
        low = lane < HEAD_DIM
        n_blk = s // QBLK

        def stacked(ref, rows):
            val = ref[rows, :]
            zero = jnp.zeros_like(val)
            return jnp.concatenate([jnp.where(low, val, zero), jnp.where(low, zero, val)], axis=0)

        for gi, d in enumerate(DILATIONS):
            @pl.when(g == gi)
            def _(d=d):
                seq = s // d
                _to_residue_major(q_rm, q_ref, s, d, cast=BF16)
                _to_residue_major(k_rm, k_ref, s, d, dst_off=HALF_SPAN, cast=BF16)
                _to_residue_major(v_rm, v_ref, s, d, dst_off=HALF_SPAN, cast=BF16)
                _to_residue_major(do_rm, do_ref, s, d, cast=BF16)
                for dst_, src_ in ((lse0_rm, lse0_p), (lse1_rm, lse1_p), (dd0_rm, dd0_p), (dd1_rm, dd1_p)):
                    _to_residue_major(dst_, src_, s, d)

                def scores(b, slot):
                    base = _block_base(b)
                    rows = pl.ds(base, QBLK)
                    win = pl.ds(base, KWIN)
                    sc = lax.dot_general(stacked(q_rm, rows), k_rm[win, :], nt_dims, preferred_element_type=F32)
                    dp = lax.dot_general(stacked(do_rm, rows), v_rm[win, :], nt_dims, preferred_element_type=F32)
                    bias = band[...] + _range_bias(base, seq)
                    for hh, (lse_r, dd_r) in enumerate(((lse0_rm, dd0_rm), (lse1_rm, dd1_rm))):
                        r = slice(hh * QBLK, (hh + 1) * QBLK)
                        lse_h = lse_r[rows, :]
                        dd_h = dd_r[rows, :]
                        sh = sc[r, :] + bias
                        p = jnp.exp(jnp.concatenate([sh[:, 0:LANES] - lse_h, sh[:, LANES:KWIN] - lse_h], axis=1))
                        dph = dp[r, :]
                        ds = p * jnp.concatenate([dph[:, 0:LANES] - dd_h, dph[:, LANES:KWIN] - dd_h], axis=1)
                        p_buf[slot, r, :] = p.astype(BF16)
                        ds_buf[slot, r, :] = ds.astype(BF16)

                def grads(b, slot):
                    base = _block_base(b)
                    rows = pl.ds(base, QBLK)
                    win = pl.ds(base, KWIN)
                    p = p_buf[slot]
                    ds = ds_buf[slot]
                    dq2 = jnp.dot(ds, k_rm[win, :], preferred_element_type=F32)
                    dq_rm[rows, :] = jnp.where(low, dq2[0:QBLK, :], dq2[QBLK:2 * QBLK, :])
                    dk_rm[win, :] += lax.dot_general(ds, stacked(q_rm, rows), tn_dims, preferred_element_type=F32)
                    dv_rm[win, :] += lax.dot_general(p, stacked(do_rm, rows), tn_dims, preferred_element_type=F32)

                _skewed_blocks(n_blk, scores, grads)
                _from_residue_major(dq_ref, dq_rm, s, d)
                _from_residue_major(dk_ref, dk_rm, s, d, src_off=HALF_SPAN)
                _from_residue_major(dv_ref, dv_rm, s, d, src_off=HALF_SPAN)

    qk_spec = pl.BlockSpec((s, LANES), lambda hp, g: (0, g * n_pairs + hp))
    v_spec = pl.BlockSpec((s, LANES), lambda hp, g: (0, v_col0 + g * n_pairs + hp))
    o_spec = pl.BlockSpec((s, LANES), lambda hp, g: (0, hp))
    width = qn.shape[1]
    f32buf = pltpu.VMEM((s, LANES), F32)
    f32pad = pltpu.VMEM((spad, LANES), F32)
    return _pallas(
        body, name=name, grid=(n_pairs, len(DILATIONS)),
        in_specs=[qk_spec, qk_spec, v_spec, o_spec, o_spec, o_spec,
                  pl.BlockSpec((LANES, LANES), lambda hp, g: (0, 0))],
        out_specs=[qk_spec, qk_spec, qk_spec],
        out_shape=[jax.ShapeDtypeStruct((s, width), F32)] * 3,
        operands=[qn, kn, proj, dattn, attn, lse, bd],
        scratch_shapes=[pltpu.VMEM((s, LANES), BF16), pltpu.VMEM((spad, LANES), BF16),
                        pltpu.VMEM((spad, LANES), BF16), pltpu.VMEM((s, LANES), BF16),
                        f32buf, f32buf, f32buf, f32buf, f32buf, f32pad, f32pad,
                        f32buf, f32buf, f32buf, f32buf, pltpu.VMEM((QBLK, KWIN), F32),
                        pltpu.VMEM((2, 2 * QBLK, KWIN), BF16), pltpu.VMEM((2, 2 * QBLK, KWIN), BF16)],
        rider=rider)


CONV_PAD = 16


def _conv_fwd(proj, conv_w, conv_b, col0, name, rider=None):
    s = proj.shape[0]
    ch = conv_w.shape[1]
    nblk = ch // LANES
    a0 = col0 // LANES
    tr = 256
    shift = CONV_PAD - (CONV_WIDTH - 1) // 2

    def body(a_ref, b_ref, w_ref, bias_ref, u0_ref, uc_ref, pad):
        z = jnp.zeros((CONV_PAD, LANES), F32)
        pad[0:CONV_PAD, :] = z
        pad[s + CONV_PAD: s + 2 * CONV_PAD, :] = z

        def glu(rows):
            u0 = a_ref[rows, :] * jax.nn.sigmoid(b_ref[rows, :])
            u0_ref[rows, :] = u0
            pad[pl.ds(rows.start + CONV_PAD, rows.size), :] = u0

        _row_chunks(s, glu)
        for t in range(0, s, tr):
            acc = jnp.broadcast_to(bias_ref[...], (tr, LANES))
            for k in range(CONV_WIDTH):
                acc = acc + w_ref[k:k + 1, :] * pad[t + k + shift: t + k + shift + tr, :]
            uc_ref[t:t + tr, :] = acc

    return _pallas(
        body, name=name, grid=(nblk,),
        in_specs=[pl.BlockSpec((s, LANES), lambda c: (0, a0 + c)),
                  pl.BlockSpec((s, LANES), lambda c: (0, a0 + nblk + c)),
                  pl.BlockSpec((CONV_WIDTH, LANES), lambda c: (0, c)),
                  pl.BlockSpec((1, LANES), lambda c: (0, c))],
        out_specs=[pl.BlockSpec((s, LANES), lambda c: (0, c))] * 2,
        out_shape=[jax.ShapeDtypeStruct((s, ch), F32)] * 2, operands=[proj, proj, conv_w, conv_b],
        scratch_shapes=[pltpu.VMEM((s + 2 * CONV_PAD, LANES), F32)], rider=rider)


def _ln_silu_fwd(uc, ln_w, ln_b, name):
    s, ch = uc.shape
    tm = 256

    def body(u_ref, w_ref, b_ref, o_ref):
        u = u_ref[...]
        mu = jnp.mean(u, axis=-1, keepdims=True)
        xc = u - mu
        rstd = lax.rsqrt(jnp.mean(xc * xc, axis=-1, keepdims=True) + EPS)
        z = xc * rstd * w_ref[...] + b_ref[...]
        o_ref[...] = (z * jax.nn.sigmoid(z)).astype(BF16)

    row = pl.BlockSpec((tm, ch), lambda i: (i, 0))
    vec = pl.BlockSpec((1, ch), lambda i: (0, 0))
    return pl.pallas_call(
        body, name=name, grid=(s // tm,), in_specs=[row, vec, vec], out_specs=row,
        out_shape=jax.ShapeDtypeStruct((s, ch), BF16), compiler_params=_params(),
    )(uc, ln_w, ln_b)


def _ln_silu_bwd(du3, uc, ln_w, ln_b, name):
    s, ch = uc.shape
    tm = 256

    def body(d_ref, u_ref, w_ref, b_ref, du_ref, dw_ref, db_ref):
        u = u_ref[...]
        mu = jnp.mean(u, axis=-1, keepdims=True)
        xc = u - mu
        rstd = lax.rsqrt(jnp.mean(xc * xc, axis=-1, keepdims=True) + EPS)
        xhat = xc * rstd
        z = xhat * w_ref[...] + b_ref[...]
        sg = jax.nn.sigmoid(z)
        dz = d_ref[...] * (sg * (1.0 + z * (1.0 - sg)))
        dxh = dz * w_ref[...]
        du_ref[...] = rstd * (dxh - jnp.mean(dxh, axis=-1, keepdims=True)
                              - xhat * jnp.mean(dxh * xhat, axis=-1, keepdims=True))
        pw = jnp.sum(dz * xhat, axis=0, keepdims=True)
        pb = jnp.sum(dz, axis=0, keepdims=True)
        first = pl.program_id(0) == 0

        @pl.when(first)
        def _():
            dw_ref[...] = pw
            db_ref[...] = pb

        @pl.when(jnp.logical_not(first))
        def _():
            dw_ref[...] += pw
            db_ref[...] += pb

    row = pl.BlockSpec((tm, ch), lambda i: (i, 0))
    vec = pl.BlockSpec((1, ch), lambda i: (0, 0))
    return pl.pallas_call(
        body, name=name, grid=(s // tm,), in_specs=[row, row, vec, vec], out_specs=[row, vec, vec],
        out_shape=[jax.ShapeDtypeStruct((s, ch), F32), jax.ShapeDtypeStruct((1, ch), F32),
                   jax.ShapeDtypeStruct((1, ch), F32)],
        compiler_params=_params(),
    )(du3, uc, ln_w, ln_b)


def _conv_bwd(duc, u0, proj, conv_w, col0, name, rider=None):
    s = proj.shape[0]
    ch = conv_w.shape[1]
    nblk = ch // LANES
    a0 = col0 // LANES
    tr = 256
    half = (CONV_WIDTH - 1) // 2
    shift = CONV_PAD - half

    def body(duc_ref, u0_ref, a_ref, b_ref, w_ref, da_ref, db_ref, dw_ref, dbias_ref, pad_d, pad_u):
        z = jnp.zeros((CONV_PAD, LANES), F32)
        for buf in (pad_d, pad_u):
            buf[0:CONV_PAD, :] = z
            buf[s + CONV_PAD: s + 2 * CONV_PAD, :] = z

        def fill(rows):
            dst = pl.ds(rows.start + CONV_PAD, rows.size)
            pad_d[dst, :] = duc_ref[rows, :]
            pad_u[dst, :] = u0_ref[rows, :]

        _row_chunks(s, fill)
        dw_acc = [jnp.zeros((8, LANES), F32) for _ in range(CONV_WIDTH)]
        dbias_acc = jnp.zeros((8, LANES), F32)
        for t in range(0, s, tr):
            d_t = duc_ref[t:t + tr, :]
            dbias_acc = dbias_acc + jnp.sum(d_t.reshape(tr // 8, 8, LANES), axis=0)
            du0 = jnp.zeros((tr, LANES), F32)
            for k in range(CONV_WIDTH):
                du0 = du0 + w_ref[k:k + 1, :] * pad_d[t - k + half + CONV_PAD: t - k + half + CONV_PAD + tr, :]
                prod = d_t * pad_u[t + k + shift: t + k + shift + tr, :]
                dw_acc[k] = dw_acc[k] + jnp.sum(prod.reshape(tr // 8, 8, LANES), axis=0)
            av = a_ref[t:t + tr, :]
            sg = jax.nn.sigmoid(b_ref[t:t + tr, :])
            da_ref[t:t + tr, :] = (du0 * sg).astype(BF16)
            db_ref[t:t + tr, :] = (du0 * av * sg * (1.0 - sg)).astype(BF16)
        for k in range(CONV_WIDTH):
            dw_ref[k:k + 1, :] = jnp.sum(dw_acc[k], axis=0, keepdims=True)
        dbias_ref[...] = jnp.sum(dbias_acc, axis=0, keepdims=True)

    col = lambda off: pl.BlockSpec((s, LANES), lambda c: (0, off + c))
    return _pallas(
        body, name=name, grid=(nblk,),
        in_specs=[col(0), col(0), col(a0), col(a0 + nblk),
                  pl.BlockSpec((CONV_WIDTH, LANES), lambda c: (0, c))],
        out_specs=[col(0), col(0), pl.BlockSpec((CONV_WIDTH, LANES), lambda c: (0, c)),
                   pl.BlockSpec((1, LANES), lambda c: (0, c))],
        out_shape=[jax.ShapeDtypeStruct((s, ch), BF16)] * 2
        + [jax.ShapeDtypeStruct((CONV_WIDTH, ch), F32), jax.ShapeDtypeStruct((1, ch), F32)],
        operands=[duc, u0, proj, proj, conv_w],
        scratch_shapes=[pltpu.VMEM((s + 2 * CONV_PAD, LANES), F32)] * 2, rider=rider)


def _mix_out_proj(attn_b, u3, w_o, w_pw, proj, bg, col0, w_out, x, norm_w, name, rider=None):
    s, d = x.shape
    k = attn_b.shape[1]
    tm = 256
    half = d // 2
    assert col0 % half == 0
    c0 = col0 // half

    def body(a_ref, u_ref, wo_ref, wp_ref, a0_ref, a1_ref, b0_ref, b1_ref, bias_ref, w_ref, x_ref, nw_ref,
             ya_ref, yb_ref, mixed_ref, x1_ref, h2_ref):
        mixed = None
        for br, (src_ref, wb_ref, lo_ref, hi_ref, y_ref) in enumerate(((a_ref, wo_ref, a0_ref, a1_ref, ya_ref),
                                                                       (u_ref, wp_ref, b0_ref, b1_ref, yb_ref))):
            src = src_ref[...]
            y = jnp.concatenate([jnp.dot(src, wb_ref[j], preferred_element_type=F32) for j in range(N_CHIPS)],
                                axis=1)
            y_ref[...] = y
            logits = jnp.concatenate([lo_ref[...], hi_ref[...]], axis=1)
            part = jax.nn.sigmoid(logits + bias_ref[br]) * y
            mixed = part if mixed is None else mixed + part
        mixed = mixed.astype(BF16)
        mixed_ref[...] = mixed
        x1 = x_ref[...] + jnp.dot(mixed, w_ref[...], preferred_element_type=F32)
        x1_ref[...] = x1
        rstd = lax.rsqrt(jnp.mean(x1 * x1, axis=-1, keepdims=True) + EPS)
        h2_ref[...] = (x1 * rstd * nw_ref[...]).astype(BF16)

    row = pl.BlockSpec((tm, d), lambda i: (i, 0))
    src_row = pl.BlockSpec((tm, k), lambda i: (i, 0))
    blocks = pl.BlockSpec(w_o.shape, lambda i: (0, 0, 0))
    logit_blk = lambda j: pl.BlockSpec((tm, half), functools.partial(lambda i, j: (i, c0 + j), j=j))
    return _pallas(
        body, name=name, grid=(s // tm,),
        in_specs=[src_row, src_row, blocks, blocks, logit_blk(0), logit_blk(1), logit_blk(2), logit_blk(3),
                  pl.BlockSpec((2, 1, d), lambda i: (0, 0, 0)), pl.BlockSpec((d, d), lambda i: (0, 0)), row,
                  pl.BlockSpec((1, d), lambda i: (0, 0))],
        out_specs=[row] * 5,
        out_shape=[jax.ShapeDtypeStruct((s, d), F32), jax.ShapeDtypeStruct((s, d), F32),
                   jax.ShapeDtypeStruct((s, d), BF16), jax.ShapeDtypeStruct((s, d), F32),
                   jax.ShapeDtypeStruct((s, d), BF16)],
        operands=[attn_b, u3, w_o, w_pw, proj, proj, proj, proj, bg, w_out, x, norm_w], rider=rider)


def _out_proj_bwd_gates(dx1, w_out, proj, bg, y_a, y_b, w_o, w_pw, col0, name, after=()):
    s, d = y_a.shape
    n_blk, k, blk = w_o.shape
    tm = 256
    half = d // 2
    assert col0 % half == 0
    c0 = col0 // half
    nt_dims = (((1,), (1,)), ((), ()))

    def body(dx_ref, w_ref, a0_ref, a1_ref, b0_ref, b1_ref, bias_ref, ya_ref, yb_ref, wo_ref, wp_ref,
             dgl_ref, dya_ref, dyb_ref, db_ref, da_ref, du_ref):
        dm = lax.dot_general(dx_ref[...], w_ref[...], nt_dims, preferred_element_type=F32)
        parts = []
        for br, (lo_ref, hi_ref, y_ref, dy_ref, wb_ref, dsrc_ref) in enumerate((
                (a0_ref, a1_ref, ya_ref, dya_ref, wo_ref, da_ref), (b0_ref, b1_ref, yb_ref, dyb_ref, wp_ref, du_ref))):
            logits = jnp.concatenate([lo_ref[...], hi_ref[...]], axis=1)
            gate = jax.nn.sigmoid(logits + bias_ref[br])
            dy = (dm * gate).astype(BF16)
            dy_ref[...] = dy
            dsrc = lax.dot_general(dy[:, 0:blk], wb_ref[0], nt_dims, preferred_element_type=F32)
            for j in range(1, n_blk):
                dsrc = dsrc + lax.dot_general(dy[:, j * blk:(j + 1) * blk], wb_ref[j], nt_dims,
                                              preferred_element_type=F32)
            dsrc_ref[...] = dsrc
            dgl = dm * y_ref[...] * gate * (1.0 - gate)
            dgl_ref[:, br * d:(br + 1) * d] = dgl.astype(BF16)
            parts.append(jnp.sum(dgl, axis=0, keepdims=True))
        part = jnp.concatenate(parts, axis=0)
        first = pl.program_id(0) == 0

        @pl.when(first)
        def _():
            db_ref[...] = part

        @pl.when(jnp.logical_not(first))
        def _():
            db_ref[...] += part

    row = pl.BlockSpec((tm, d), lambda i: (i, 0))
    logit_blk = lambda k: pl.BlockSpec((tm, half), functools.partial(lambda i, k: (i, c0 + k), k=k))
    blocks = pl.BlockSpec(w_o.shape, lambda i: (0, 0, 0))
    src_row = pl.BlockSpec((tm, k), lambda i: (i, 0))
    return _pallas(
        body, name=name, grid=(s // tm,),
        in_specs=[row, pl.BlockSpec((d, d), lambda i: (0, 0)), logit_blk(0), logit_blk(1), logit_blk(2),
                  logit_blk(3), pl.BlockSpec((2, 1, d), lambda i: (0, 0, 0)), row, row, blocks, blocks],
        out_specs=[pl.BlockSpec((tm, 2 * d), lambda i: (i, 0)), row, row, pl.BlockSpec((2, d), lambda i: (0, 0)),
                   src_row, src_row],
        out_shape=[jax.ShapeDtypeStruct((s, 2 * d), BF16), jax.ShapeDtypeStruct((s, d), BF16),
                   jax.ShapeDtypeStruct((s, d), BF16), jax.ShapeDtypeStruct((2, d), F32),
                   jax.ShapeDtypeStruct((s, k), F32), jax.ShapeDtypeStruct((s, k), F32)],
        operands=[dx1, w_out, proj, proj, proj, proj, bg, y_a, y_b, w_o, w_pw], after=after)


def _ffn_in_swiglu(h2, w_blocked, name):
    s, k = h2.shape
    nblk, _, tn = w_blocked.shape
    ff = nblk // 2 * tn
    tm = 512

    def body(a_ref, wg_ref, wu_ref, g_ref, u_ref, act_ref):
        a = a_ref[...]
        gt = jnp.dot(a, wg_ref[...], preferred_element_type=F32)
        up = jnp.dot(a, wu_ref[...], preferred_element_type=F32)
        g_ref[...] = gt
        u_ref[...] = up
        act_ref[...] = (gt * jax.nn.sigmoid(gt) * up).astype(BF16)

    out = pl.BlockSpec((tm, tn), lambda j, i: (i, j))
    return pl.pallas_call(
        body, name=name, grid=(nblk // 2, s // tm),
        in_specs=[pl.BlockSpec((tm, k), lambda j, i: (i, 0)),
                  pl.BlockSpec((None, k, tn), lambda j, i: (j, 0, 0)),
                  pl.BlockSpec((None, k, tn), lambda j, i: (nblk // 2 + j, 0, 0))],
        out_specs=[out, out, out],
        out_shape=[jax.ShapeDtypeStruct((s, ff), F32), jax.ShapeDtypeStruct((s, ff), F32),
                   jax.ShapeDtypeStruct((s, ff), BF16)],
        compiler_params=_params(),
    )(h2, w_blocked, w_blocked)


def _ffn_out_bwd_swiglu(dy, w_ffn_out, gate, up, name, rider=None):
    s, d = dy.shape
    ff = gate.shape[1]
    tm = 256
    nt_dims = (((1,), (1,)), ((), ()))

    def body(dy_ref, w_ref, g_ref, u_ref, o_ref):
        dv = lax.dot_general(dy_ref[...], w_ref[...], nt_dims, preferred_element_type=F32)
        gt = g_ref[...]
        sg = jax.nn.sigmoid(gt)
        o_ref[:, 0:ff] = (dv * u_ref[...] * (sg * (1.0 + gt * (1.0 - sg)))).astype(BF16)
        o_ref[:, ff:2 * ff] = (dv * gt * sg).astype(BF16)

    row = pl.BlockSpec((tm, ff), lambda i: (i, 0))
    return _pallas(
        body, name=name, grid=(s // tm,),
        in_specs=[pl.BlockSpec((tm, d), lambda i: (i, 0)), pl.BlockSpec((ff, d), lambda i: (0, 0)), row, row],
        out_specs=pl.BlockSpec((tm, 2 * ff), lambda i: (i, 0)),
        out_shape=jax.ShapeDtypeStruct((s, 2 * ff), BF16), operands=[dy, w_ffn_out, gate, up], rider=rider)


def _ffn_out_loss(act, w_ffn_out, x1, target, name):
    s, k = act.shape
    d = w_ffn_out.shape[1]
    tm = 512

    def body(a_ref, w_ref, x1_ref, t_ref, dy_ref, dyb_ref, loss_ref, acc):
        y = x1_ref[...] + jnp.dot(a_ref[...], w_ref[...], preferred_element_type=F32)
        diff = y - t_ref[...]
        dy = diff * (1.0 / d)
        dy_ref[...] = dy
        dyb_ref[...] = dy.astype(BF16)
        part = jnp.sum((diff * diff).reshape(tm // 8, 8, d), axis=0)
        i = pl.program_id(0)

        @pl.when(i == 0)
        def _():
            acc[...] = part

        @pl.when(i > 0)
        def _():
            acc[...] += part

        @pl.when(i == pl.num_programs(0) - 1)
        def _():
            loss_ref[...] = (0.5 / d) * jnp.sum(jnp.sum(acc[...], axis=1, keepdims=True), axis=0, keepdims=True)

    row = pl.BlockSpec((tm, d), lambda i: (i, 0))
    return pl.pallas_call(
        body, name=name, grid=(s // tm,),
        in_specs=[pl.BlockSpec((tm, k), lambda i: (i, 0)), pl.BlockSpec((k, d), lambda i: (0, 0)), row, row],
        out_specs=[row, row, pl.BlockSpec((1, 1), lambda i: (0, 0))],
        out_shape=[jax.ShapeDtypeStruct((s, d), F32), jax.ShapeDtypeStruct((s, d), BF16),
                   jax.ShapeDtypeStruct((1, 1), F32)],
        scratch_shapes=[pltpu.VMEM((8, d), F32)], compiler_params=_params(),
    )(act, w_ffn_out, x1, target)


LATE_GATHER = ("w_o_attn", "w_pw_conv", "w_out", "w_ffn_in", "w_ffn_out")
EARLY_REDUCE = LATE_GATHER


def _blocks_by_half(g):
    if g.ndim == 2:
        g = g.reshape(N_CHIPS, g.shape[0] // N_CHIPS, g.shape[1])
    return g.reshape(N_CHIPS, 2, g.shape[1] // 2, g.shape[2])


def _forward_backward(x, pos_col, target, wts, first_gather, late_bufs, pos_arr):
    wts = dict(wts)
    consts = _rope_consts()
    bd = consts[3]
    qw2 = jnp.tile(wts["q_norm_w"], (1, LANES // HEAD_DIM))
    kw2 = jnp.tile(wts["k_norm_w"], (1, LANES // HEAD_DIM))
    qkv_w = 3 * N_SLOT_HEADS * HEAD_DIM
    conv_col0 = 3 * qkv_w

    h = _rmsnorm_fwd(x, wts["norm1_w"], "rms1_fwd")
    slot_order = jnp.bitwise_xor(pos_arr[1], jnp.asarray([0, 2, 1, 3], jnp.int32))
    blocked = lambda buf: buf.reshape(N_CHIPS, -1, buf.shape[3])
    near = first_gather
    proj = _proj_by_slot(h, blocked(near[2][0]), slot_order, 0, 1, "mm_proj_own", after=list(late_bufs))
    near = _split_middle(near, after=[proj], name="allgather_w_in_near_forward")
    far, _ = _split_start(_gather_both_legs_rider(near[2][:1], near[2][1:], FAR_CHIPS), "allgather_w_in_far_start")
    _, bufs = _split_wait((near[0], near[1], far[2], near[3]), after=[], name="allgather_w_in_near_wait")
    proj = _proj_by_slot(h, blocked(bufs[0]), slot_order, 1, len(NEAR_CHIPS), "mm_proj_near", proj=proj)
    far = _split_middle((far[0], far[1], bufs, far[3]), after=[proj], name="allgather_w_in_far_forward")
    (w_in_buf, conv_w_buf, b_gate_buf), _ = _split_wait(far, after=[], name="allgather_w_in_far_wait")
    wts["w_in"] = w_in_buf.reshape(N_CHIPS, -1, w_in_buf.shape[3])
    wts["conv_w"] = conv_w_buf.transpose(1, 0, 2).reshape(CONV_WIDTH, -1)
    wts["b_gate"] = b_gate_buf.transpose(1, 0, 2).reshape(2, 1, -1)
    ch = wts["conv_w"].shape[1]
    gate_col0 = conv_col0 + 2 * ch
    n_mix = LATE_GATHER.index("w_ffn_in")
    mix_gather, started = _split_start(_gather_ici_rider(late_bufs[:n_mix], []), "late_gather_mix_start",
                                       after=[wts["w_in"]])
    ffn_gather, started = _split_start(_gather_ici_rider(late_bufs[n_mix:], []), "late_gather_ffn_start",
                                       after=[started])
    proj = _proj_by_slot(h, wts["w_in"], slot_order, 1 + len(NEAR_CHIPS), len(FAR_CHIPS), "mm_proj_far", proj=proj,
                         after=[started])
    qn, kn = _qk_fwd(proj, pos_col, qw2, kw2, consts, "qk_fwd")
    attn, lse, attn_b = _attn_fwd(qn, kn, proj, "attn_fwd")

    def gathered(names, bufs):
        for n, buf in zip(names, bufs):
            full = buf.reshape(N_CHIPS, -1, buf.shape[3])
            wts[n] = full.reshape(-1, full.shape[2]) if n in ROW_SHARDED else full

    mix_bufs, _ = _split_wait(mix_gather, after=[attn_b], name="late_gather_mix_wait")
    (u0, uc), mix_bufs = _conv_fwd(proj, wts["conv_w"], wts["conv_b"], conv_col0, "conv_fwd",
                                   rider=_gather_forward_rider(mix_bufs))
    gathered(LATE_GATHER[:n_mix], mix_bufs)
    u3 = _ln_silu_fwd(uc, wts["conv_ln_w"], wts["conv_ln_b"], "ln_fwd")
    ffn_bufs, _ = _split_wait(ffn_gather, after=[u3], name="late_gather_ffn_wait")
    (y_a, y_b, mixed, x1, h2), ffn_bufs = _mix_out_proj(
        attn_b, u3, wts["w_o_attn"], wts["w_pw_conv"], proj, wts["b_gate"], gate_col0, wts["w_out"], x,
        wts["norm2_w"], "mix_x1_rms2", rider=_gather_forward_rider(ffn_bufs))
    gathered(LATE_GATHER[n_mix:], ffn_bufs)
    gate, up, act = _ffn_in_swiglu(h2, wts["w_ffn_in"], "mm_gu_swiglu")
    dy, dy_b16, loss = _ffn_out_loss(act, wts["w_ffn_out"], x1, target, "mm_x2_loss")

    g = {}
    by_chip = {}

    def pair_add(n, blocks, received):
        return _add_own_half(blocks, received, pos_arr, f"grads_pair_add_{n}")

    g_ffn_out = _blocks_by_half(
        _matmul(act, dy_b16, mode="tn", tm=1408, tn=1024, tk=2048, out_dtype=F32, name="mm_dwffnout"))
    dgu, (received,) = _ffn_out_bwd_swiglu(dy_b16, wts["w_ffn_out"], gate, up, "mm_dact_swiglu_bwd",
                                           rider=_pair_exchange_rider([g_ffn_out], halved=True))
    to_send, own = pair_add("w_ffn_out", g_ffn_out, received)
    (dx1, dx1_b16, g["norm2_w"]), (by_chip["w_ffn_out"],) = _matmul_nt_rmsnorm_bwd(
        dgu, wts["w_ffn_in"], x1, wts["norm2_w"], dy, "mm_dh2_rms2_bwd", rider=_chip_exchange_rider([to_send], [own]))
    g_ffn_in = _blocks_by_half(_matmul(h2, dgu, mode="tn", tm=512, tn=1408, tk=2048, out_dtype=F32,
                                       name="mm_dwffnin", out_blocked=N_CHIPS, cols_outer=True))
    exchanging, started = _split_start(_pair_exchange_rider([g_ffn_in], halved=True), "grads_ffn_in_pair_start")
    g["w_out"] = _matmul(mixed, dx1_b16, mode="tn", tm=512, tn=1024, tk=2048, out_dtype=F32, name="mm_dwout",
                         after=[started])
    dgl, dy_a, dy_b, g["b_gate"], dattn, du3 = _out_proj_bwd_gates(
        dx1_b16, wts["w_out"], proj, wts["b_gate"], y_a, y_b, wts["w_o_attn"], wts["w_pw_conv"], gate_col0,
        "mix_bwd")
    (received,), (g_ffn_in,) = _split_wait(exchanging, after=[dgl], name="grads_ffn_in_pair_wait")
    ffn_in_to_send, ffn_in_own = pair_add("w_ffn_in", g_ffn_in, received)
    g["w_o_attn"] = _matmul(attn_b, dy_a, mode="tn", tm=512, tn=256, tk=2048, out_dtype=F32, name="mm_dwo",
                            out_blocked=N_CHIPS)
    g["w_pw_conv"] = _matmul(u3, dy_b, mode="tn", tm=512, tn=256, tk=2048, out_dtype=F32, name="mm_dwpw",
                             out_blocked=N_CHIPS)
    duc, g["conv_ln_w"], g["conv_ln_b"] = _ln_silu_bwd(du3, uc, wts["conv_ln_w"], wts["conv_ln_b"], "ln_bwd")

    small3 = ("w_out", "w_o_attn", "w_pw_conv")
    g_small3 = [_blocks_by_half(g.pop(n)) for n in small3]
    (da, db, g["conv_w"], g["conv_b"]), received = _conv_bwd(
        duc, u0, proj, wts["conv_w"], conv_col0, "conv_bwd", rider=_pair_exchange_rider(g_small3, halved=True))
    sums3 = [pair_add(n, gb, rv) for n, gb, rv in zip(small3, g_small3, received)]
    (dqn, dkn, dv), (by_chip["w_ffn_in"],) = _attn_bwd(
        qn, kn, proj, dattn, attn, lse, bd, "attn_bwd",
        rider=_chip_exchange_rider([ffn_in_to_send], [ffn_in_own]))
    (dproj, dqw, dkw), exchanged3 = _qk_bwd(
        dqn, dkn, dv, da, db, dgl, proj, pos_col, qw2, kw2, consts, "qk_bwd",
        rider=_chip_exchange_rider([s[0] for s in sums3], [s[1] for s in sums3]))
    by_chip.update(zip(small3, exchanged3))
    halves = [_sum_chips(by_chip[n], pos_arr, f"grads_chip_sum_{n}") for n in EARLY_REDUCE]
    g["q_norm_w"] = dqw[:, :HEAD_DIM]
    g["k_norm_w"] = dkw[:, :HEAD_DIM]

    c = pos_arr[0]
    rh = h.shape[1] // 2
    h_sibling = lax.dynamic_slice_in_dim(h, (1 - c) * rh, rh, axis=1)
    h_own = lax.dynamic_slice_in_dim(h, c * rh, rh, axis=1)
    g_sibling, shards = _matmul(h_sibling, dproj, mode="tn", tm=rh, tn=1920, tk=2048, out_dtype=F32,
                                name="mm_dwin_sibling", out_blocked=N_CHIPS, rider=_pair_gather_rider(halves))
    reduced = dict(zip(EARLY_REDUCE, shards))
    exchanging, started = _split_start(_pair_exchange_rider([g_sibling], halved=False), "grads_w_in_pair_start")
    g_own = _matmul(h_own, dproj, mode="tn", tm=rh, tn=1920, tk=2048, out_dtype=F32, name="mm_dwin_own",
                    out_blocked=N_CHIPS, after=[started])
    (from_sibling,), _ = _split_wait(exchanging, after=[g_own], name="grads_w_in_pair_wait")
    to_send, own = _add_own_half(g_own, from_sibling, pos_arr, "grads_pair_add_w_in")
    in_flight, started = _split_start(_chip_exchange_rider([to_send], [own]), "grads_w_in_exchange_start")
    grad_x, _, g["norm1_w"] = _matmul_nt_rmsnorm_bwd(dproj, wts["w_in"], x, wts["norm1_w"], dx1, "mm_dh_rms1_bwd",
                                                     after=[started])
    return loss, grad_x, g, reduced, (in_flight, started)


def _mesh_pos():
    return lax.axis_index("x"), lax.axis_index("y"), lax.axis_index("c")


def _other_chips(x, y):
    return [(1 - x, y), (x, 1 - y), (1 - x, 1 - y)]


NEAR_CHIPS, FAR_CHIPS = (0, 1), (2,)


def _cast_into_slot(shard, chip_arr, dtype, name, n_slots=N_CHIPS, after=()):
    r, c = shard.shape
    tr = r // 2 if r % 32 == 0 else r

    def body(chip_ref, s_ref, *refs):
        refs[-1][...] = s_ref[...].astype(dtype)

    return pl.pallas_call(
        body, name=name,
        grid_spec=pltpu.PrefetchScalarGridSpec(
            num_scalar_prefetch=1, grid=(r // tr,),
            in_specs=[pl.BlockSpec((tr, c), lambda i, chip_ref: (i, 0))] + [ANY] * len(after),
            out_specs=pl.BlockSpec((None, tr, c), lambda i, chip_ref: (chip_ref[0], i, 0))),
        out_shape=jax.ShapeDtypeStruct((n_slots, r, c), dtype), compiler_params=_params(),
    )(chip_arr, shard, *after)


GATHER_CHUNKS = 4


def _gather_both_legs_rider(big, small, peers):
    nb = len(big)
    n = nb + len(small)
    nch = GATHER_CHUNKS

    def part(bufs, a, slot, half, ch):
        if a >= nb:
            return bufs[a].at[slot]
        rows = bufs[a].shape[2] // nch
        return bufs[a].at[slot, half, pl.ds(ch * rows, rows)]

    def pieces():
        return [(a, ch, k) for ch in range(nch) for a in range(n) for k in peers if a < nb or ch == 0]

    def ici(bufs, sems, a, ch, k, slot_of_src):
        x, y, c = _mesh_pos()
        px, py = _other_chips(x, y)[k]
        slot = 2 * x + y if slot_of_src == "mine" else 2 * px + py
        return pltpu.make_async_remote_copy(
            src_ref=part(bufs, a, slot, c, ch), dst_ref=part(bufs, a, slot, c, ch), send_sem=sems[0].at[a, ch, k],
            recv_sem=sems[1].at[a, ch, k], device_id=(px, py, c), device_id_type=MESH)

    def forward(bufs, sems, a, ch, k, half):
        x, y, c = _mesh_pos()
        px, py = _other_chips(x, y)[k]
        h = c if half == "mine" else 1 - c
        return pltpu.make_async_remote_copy(
            src_ref=part(bufs, a, 2 * px + py, h, ch), dst_ref=part(bufs, a, 2 * px + py, h, ch),
            send_sem=sems[2].at[a, ch, k], recv_sem=sems[3].at[a, ch, k], device_id=(x, y, 1 - c),
            device_id_type=MESH)

    def start(r_in, bufs, sems):
        for a, ch, k in pieces():
            ici(bufs, sems, a, ch, k, "mine").start()

    def middle(r_in, bufs, sems):
        for a, ch, k in pieces():
            ici(bufs, sems, a, ch, k, "theirs").wait_recv()
            if a < nb:
                forward(bufs, sems, a, ch, k, "mine").start()
        for a, ch, k in pieces():
            ici(bufs, sems, a, ch, k, "mine").wait_send()

    def wait(r_in, bufs, sems):
        for a, ch, k in pieces():
            if a < nb:
                forward(bufs, sems, a, ch, k, "theirs").wait_recv()
        for a, ch, k in pieces():
            if a < nb:
                forward(bufs, sems, a, ch, k, "mine").wait_send()

    ops = list(big) + list(small)
    return _Rider(ops, [jax.ShapeDtypeStruct(o.shape, o.dtype) for o in ops], {i: i for i in range(n)},
                  [pltpu.SemaphoreType.DMA((n, nch, 3)), pltpu.SemaphoreType.DMA((n, nch, 3)),
                   pltpu.SemaphoreType.DMA((nb, nch, 3)), pltpu.SemaphoreType.DMA((nb, nch, 3))],
                  start, wait, middle)


def _comm_call(rider, name):
    def body():
        pass

    return _pallas(body, name=name, grid=(1,), in_specs=[], out_specs=[], out_shape=[], operands=[],
                   rider=rider)[1]


def _gather_ici_rider(big, small):
    nb = len(big)
    n = nb + len(small)

    def copies(bufs, sems):
        x, y, c = _mesh_pos()
        me = 2 * x + y
        part = lambda a, slot: bufs[a].at[slot, c] if a < nb else bufs[a].at[slot]
        out = []
        for a in range(n):
            for k, (px, py) in enumerate(_other_chips(x, y)):
                send = functools.partial(
                    pltpu.make_async_remote_copy,
                    src_ref=part(a, me), dst_ref=part(a, me), send_sem=sems[0].at[a, k],
                    recv_sem=sems[1].at[a, k], device_id=(px, py, c), device_id_type=MESH)
                recv = functools.partial(
                    pltpu.make_async_remote_copy,
                    src_ref=part(a, 2 * px + py), dst_ref=part(a, 2 * px + py), send_sem=sems[0].at[a, k],
                    recv_sem=sems[1].at[a, k], device_id=(px, py, c), device_id_type=MESH)
                out.append((send, recv))
        return out

    def start(r_in, r_out, sems):
        for send, _ in copies(r_out, sems):
            send().start()

    def wait(r_in, r_out, sems):
        cps = copies(r_out, sems)
        for _, recv in cps:
            recv().wait_recv()
        for send, _ in cps:
            send().wait_send()

    ops = list(big) + list(small)
    return _Rider(ops, [jax.ShapeDtypeStruct(o.shape, o.dtype) for o in ops], {i: i for i in range(n)},
                  [pltpu.SemaphoreType.DMA((n, 3)), pltpu.SemaphoreType.DMA((n, 3))], start, wait)


def _gather_forward_rider(big):
    n = len(big)

    def copies(bufs, sems):
        x, y, c = _mesh_pos()
        out = []
        for a in range(n):
            for k, (px, py) in enumerate(_other_chips(x, y)):
                slot = 2 * px + py
                send = functools.partial(
                    pltpu.make_async_remote_copy,
                    src_ref=bufs[a].at[slot, c], dst_ref=bufs[a].at[slot, c], send_sem=sems[0].at[a, k],
                    recv_sem=sems[1].at[a, k], device_id=(x, y, 1 - c), device_id_type=MESH)
                recv = functools.partial(
                    pltpu.make_async_remote_copy,
                    src_ref=bufs[a].at[slot, 1 - c], dst_ref=bufs[a].at[slot, 1 - c], send_sem=sems[0].at[a, k],
                    recv_sem=sems[1].at[a, k], device_id=(x, y, 1 - c), device_id_type=MESH)
                out.append((send, recv))
        return out

    def start(r_in, r_out, sems):
        for send, _ in copies(r_out, sems):
            send().start()

    def wait(r_in, r_out, sems):
        cps = copies(r_out, sems)
        for _, recv in cps:
            recv().wait_recv()
        for send, _ in cps:
            send().wait_send()

    return _Rider(big, [jax.ShapeDtypeStruct(o.shape, o.dtype) for o in big], {i: i for i in range(n)},
                  [pltpu.SemaphoreType.DMA((n, 3)), pltpu.SemaphoreType.DMA((n, 3))], start, wait)


def _pair_exchange_rider(gs, halved):
    n = len(gs)

    def copies(r_in, r_out, sems):
        x, y, c = _mesh_pos()
        return [pltpu.make_async_remote_copy(
            src_ref=r_in[a].at[:, 1 - c] if halved else r_in[a], dst_ref=r_out[a], send_sem=sems[0].at[a],
            recv_sem=sems[1].at[a], device_id=(x, y, 1 - c), device_id_type=MESH) for a in range(n)]

    def start(r_in, r_out, sems):
        for cp in copies(r_in, r_out, sems):
            cp.start()

    def wait(r_in, r_out, sems):
        for cp in copies(r_in, r_out, sems):
            cp.wait()

    return _Rider(gs, [jax.ShapeDtypeStruct((g.shape[0],) + g.shape[-2:], g.dtype) for g in gs], {},
                  [pltpu.SemaphoreType.DMA((n,)), pltpu.SemaphoreType.DMA((n,))], start, wait)


def _chip_exchange_rider(to_send, by_chip, row_range=None):
    n = len(to_send)

    def copies(r_in, r_out, sems):
        x, y, c = _mesh_pos()
        me = 2 * x + y
        rows = (lambda ref: ref) if row_range is None else (lambda ref: ref.at[pl.ds(*row_range)])
        out = []
        for a in range(n):
            for k, (px, py) in enumerate(_other_chips(x, y)):
                send = functools.partial(
                    pltpu.make_async_remote_copy,
                    src_ref=rows(r_in[a].at[2 * px + py]), dst_ref=rows(r_out[a].at[me]),
                    send_sem=sems[0].at[a, k], recv_sem=sems[1].at[a, k], device_id=(px, py, c),
                    device_id_type=MESH)
                recv = functools.partial(
                    pltpu.make_async_remote_copy,
                    src_ref=rows(r_in[a].at[me]), dst_ref=rows(r_out[a].at[2 * px + py]),
                    send_sem=sems[0].at[a, k], recv_sem=sems[1].at[a, k], device_id=(px, py, c),
                    device_id_type=MESH)
                out.append((send, recv))
        return out

    def start(r_in, r_out, sems):
        for send, _ in copies(r_in, r_out, sems):
            send().start()

    def wait(r_in, r_out, sems):
        cps = copies(r_in, r_out, sems)
        for _, recv in cps:
            recv().wait_recv()
        for send, _ in cps:
            send().wait_send()

    return _Rider(list(to_send) + list(by_chip), [jax.ShapeDtypeStruct(b.shape, b.dtype) for b in by_chip],
                  {n + i: i for i in range(n)},
                  [pltpu.SemaphoreType.DMA((n, 3)), pltpu.SemaphoreType.DMA((n, 3))], start, wait)


HBM = pl.BlockSpec(memory_space=pltpu.HBM)
SEM = pl.BlockSpec(memory_space=pltpu.SEMAPHORE)


_IN_FLIGHT = pltpu.CompilerParams(has_side_effects=pltpu.SideEffectType.DATAFLOW_SIDE_EFFECTING)


class _FlatSems:
    def __init__(self, ref, shape):
        self.ref, self.shape = ref, shape

    @property
    def at(self):
        return self

    def __getitem__(self, idx):
        idx = idx if isinstance(idx, tuple) else (idx,)
        flat = 0
        for i, n in zip(idx, self.shape):
            flat = flat * n + i
        return self.ref.at[flat]


def _flat_sem_types(rider):
    return tuple(pltpu.SemaphoreType.DMA((int(np.prod(s.shape)),)) for s in rider.scratch)


def _as_rider_sems(rider, refs):
    return [_FlatSems(r, s.shape) for r, s in zip(refs, rider.scratch)]


def _split_start(rider, name, after=()):
    n_in, n_out, n_sem = len(rider.operands), len(rider.out_shapes), len(rider.scratch)
    n_after = len(after)
    fresh = [j for j in range(n_out) if j not in rider.aliases.values()]
    by_out = {j: i for i, j in rider.aliases.items()}

    def body(*refs):
        r_in = refs[:n_in]
        refs = refs[n_in + n_after:]
        sems = refs[:n_sem]
        thru = refs[n_sem:n_sem + n_in]
        fresh_refs = refs[n_sem + n_in:n_sem + n_in + len(fresh)]
        token = refs[-1]
        r_out = [thru[by_out[j]] if j in by_out else fresh_refs[fresh.index(j)] for j in range(n_out)]
        rider.start(r_in, r_out, _as_rider_sems(rider, sems))
        token[...] = jnp.zeros_like(token)

    res = pl.pallas_call(
        body, name=name,
        out_shape=_flat_sem_types(rider) + tuple(pltpu.HBM(o.shape, o.dtype) for o in rider.operands)
        + tuple(pltpu.HBM(rider.out_shapes[j].shape, rider.out_shapes[j].dtype) for j in fresh)
        + (jax.ShapeDtypeStruct((8, LANES), F32),),
        in_specs=(HBM,) * n_in + (ANY,) * n_after,
        out_specs=(SEM,) * n_sem + (HBM,) * (n_in + len(fresh)) + (pl.BlockSpec(memory_space=pltpu.VMEM),),
        input_output_aliases={i: n_sem + i for i in range(n_in)}, compiler_params=_IN_FLIGHT,
    )(*[pltpu.with_memory_space_constraint(o, pltpu.HBM) for o in rider.operands], *after)
    return (rider, res[:n_sem], res[n_sem:n_sem + n_in], res[n_sem + n_in:-1]), res[-1]


def _split_continue(handles, after, name, phase):
    rider, sems, thru, fresh_arrays = handles
    n_in, n_out, n_sem = len(rider.operands), len(rider.out_shapes), len(rider.scratch)
    fresh = [j for j in range(n_out) if j not in rider.aliases.values()]
    by_out = {j: i for i, j in rider.aliases.items()}
    n_data = n_in + len(fresh)

    def body(*refs):
        r_in = refs[:n_in]
        fresh_refs = refs[n_in:n_data]
        sem_refs = refs[n_data:n_data + n_sem]
        r_out = [r_in[by_out[j]] if j in by_out else fresh_refs[fresh.index(j)] for j in range(n_out)]
        phase(r_in, r_out, _as_rider_sems(rider, sem_refs))

    data = list(thru) + list(fresh_arrays)
    return pl.pallas_call(
        body, name=name, out_shape=tuple(pltpu.HBM(d.shape, d.dtype) for d in data),
        in_specs=(HBM,) * n_data + (SEM,) * n_sem + (ANY,) * len(after), out_specs=(HBM,) * n_data,
        input_output_aliases={i: i for i in range(n_data)}, compiler_params=_IN_FLIGHT,
    )(*data, *sems, *after)


def _split_middle(handles, after, name):
    rider, sems, thru, _ = handles
    res = _split_continue(handles, after, name, rider.middle)
    return rider, sems, res[:len(thru)], res[len(thru):]


def _split_wait(handles, after, name):
    rider = handles[0]
    n_in, n_out = len(rider.operands), len(rider.out_shapes)
    fresh = [j for j in range(n_out) if j not in rider.aliases.values()]
    by_out = {j: i for i, j in rider.aliases.items()}
    res = _split_continue(handles, after, name, rider.wait)
    return [res[by_out[j]] if j in by_out else res[n_in + fresh.index(j)] for j in range(n_out)], res[:n_in]


def _pair_gather_rider(bufs):
    n = len(bufs)

    def copies(r_out, sems):
        x, y, c = _mesh_pos()
        out = []
        for a in range(n):
            send = functools.partial(
                    pltpu.make_async_remote_copy,
                src_ref=r_out[a].at[c], dst_ref=r_out[a].at[c], send_sem=sems[0].at[a],
                recv_sem=sems[1].at[a], device_id=(x, y, 1 - c), device_id_type=MESH)
            recv = functools.partial(
                    pltpu.make_async_remote_copy,
                src_ref=r_out[a].at[1 - c], dst_ref=r_out[a].at[1 - c], send_sem=sems[0].at[a],
                recv_sem=sems[1].at[a], device_id=(x, y, 1 - c), device_id_type=MESH)
            out.append((send, recv))
        return out

    def start(r_in, r_out, sems):
        for send, _ in copies(r_out, sems):
            send().start()

    def wait(r_in, r_out, sems):
        cps = copies(r_out, sems)
        for _, recv in cps:
            recv().wait_recv()
        for send, _ in cps:
            send().wait_send()

    return _Rider(bufs, [jax.ShapeDtypeStruct(b.shape, b.dtype) for b in bufs], {i: i for i in range(n)},
                  [pltpu.SemaphoreType.DMA((n,)), pltpu.SemaphoreType.DMA((n,))], start, wait)


def _add_own_half(g, recv, pos_arr, name):
    nb, rh, cols = g.shape[0], g.shape[-2], g.shape[-1]

    def body(pos_ref, g_ref, r_ref, send_ref, own_ref):
        s = (g_ref[...] + r_ref[...]).astype(BF16)
        send_ref[...] = s

        @pl.when(pl.program_id(0) == pos_ref[1])
        def _():
            own_ref[...] = s

    blk = pl.BlockSpec((None, rh, cols), lambda j, pos_ref: (j, 0, 0))
    g_spec = blk if g.ndim == 3 else pl.BlockSpec((None, None, rh, cols),
                                                   lambda j, pos_ref: (j, pos_ref[0], 0, 0))
    shape = jax.ShapeDtypeStruct((nb, rh, cols), BF16)
    return pl.pallas_call(
        body, name=name,
        grid_spec=pltpu.PrefetchScalarGridSpec(
            num_scalar_prefetch=1, grid=(nb,), in_specs=[g_spec, blk],
            out_specs=[blk, pl.BlockSpec((None, rh, cols), lambda j, pos_ref: (pos_ref[1], 0, 0))]),
        out_shape=[shape, shape], compiler_params=_params(),
    )(pos_arr, g, recv)


def _sum_chips(gath, pos_arr, name):
    nb, rh, cols = gath.shape

    def body(pos_ref, a_ref, b_ref, c_ref, d_ref, o_ref):
        del pos_ref
        o_ref[...] = ((a_ref[...].astype(F32) + b_ref[...].astype(F32)) + c_ref[...].astype(F32)) \
            + d_ref[...].astype(F32)

    tr = rh // 2 if (rh // 2) % 16 == 0 else rh
    specs = [pl.BlockSpec((None, tr, cols), functools.partial(lambda i, pos_ref, j: (j, i, 0), j=j))
             for j in range(nb)]
    return pl.pallas_call(
        body, name=name,
        grid_spec=pltpu.PrefetchScalarGridSpec(
            num_scalar_prefetch=1, grid=(rh // tr,), in_specs=specs,
            out_specs=pl.BlockSpec((None, tr, cols), lambda i, pos_ref: (pos_ref[0], i, 0))),
        out_shape=jax.ShapeDtypeStruct((2, rh, cols), F32), compiler_params=_params(),
    )(pos_arr, gath, gath, gath, gath)


N_DEVICES = 8


def _small_gather_rider(buf):
    def copies(r_out, sems):
        x, y, c = _mesh_pos()
        me = 4 * x + 2 * y + c
        out = []
        for r in range(1, N_DEVICES):
            px = 1 - x if r & 4 else x
            py = 1 - y if r & 2 else y
            pc = 1 - c if r & 1 else c
            out.append(pltpu.make_async_remote_copy(
                src_ref=r_out[0].at[me], dst_ref=r_out[0].at[me], send_sem=sems[0].at[r - 1],
                recv_sem=sems[1].at[r - 1], device_id=(px, py, pc), device_id_type=MESH))
        return out

    def start(r_in, r_out, sems):
        for cp in copies(r_out, sems):
            cp.start()

    def wait(r_in, r_out, sems):
        cps = copies(r_out, sems)
        for cp in cps:
            cp.wait_recv()
        for cp in cps:
            cp.wait_send()

    return _Rider([buf], [jax.ShapeDtypeStruct(buf.shape, buf.dtype)], {0: 0},
                  [pltpu.SemaphoreType.DMA((N_DEVICES - 1,)), pltpu.SemaphoreType.DMA((N_DEVICES - 1,))],
                  start, wait)


def _sum_devices(buf, name):
    def body(b_ref, o_ref):
        acc = b_ref[0]
        for i in range(1, N_DEVICES):
            acc = acc + b_ref[i]
        o_ref[...] = acc

    return _pallas(body, name=name, grid=(1,), in_specs=[pl.BlockSpec(buf.shape, lambda i: (0, 0, 0))],
                   out_specs=pl.BlockSpec(buf.shape[1:], lambda i: (0, 0)),
                   out_shape=jax.ShapeDtypeStruct(buf.shape[1:], F32), operands=[buf])


def _adamw_math(w, g, m, v):
    m = ADAM_B1 * m + (1.0 - ADAM_B1) * g
    v = ADAM_B2 * v + (1.0 - ADAM_B2) * (g * g)
    m_hat = m / (1.0 - ADAM_B1 ** ADAM_STEP)
    v_hat = v / (1.0 - ADAM_B2 ** ADAM_STEP)
    delta = -ADAM_LR * (m_hat / (jnp.sqrt(v_hat) + ADAM_EPS) + ADAM_WD * w)
    return delta, m, v


def _adamw(w, g, m, v, name, after=()):
    r, c = w.shape
    tr = 128 if r % 128 == 0 else 64
    assert r % tr == 0

    def body(w_ref, g_ref, m_ref, v_ref, go_ref, d_ref, mo_ref, vo_ref):
        gv = g_ref[...]
        d, mn, vn = _adamw_math(w_ref[...], gv, m_ref[...], v_ref[...])
        go_ref[...] = gv
        d_ref[...] = d
        mo_ref[...] = mn
        vo_ref[...] = vn

    blk = pl.BlockSpec((tr, c), lambda i: (i, 0))
    return _pallas(body, name=name, grid=(r // tr,), in_specs=[blk] * 4, out_specs=[blk] * 4,
                   out_shape=[jax.ShapeDtypeStruct((r, c), F32)] * 4, operands=[w, g, m, v], after=after)


def _adamw_small(ws, gs, ms, vs, name):
    n = len(ws)

    def body(*refs):
        w_r, g_r, m_r, v_r = refs[:n], refs[n:2 * n], refs[2 * n:3 * n], refs[3 * n:4 * n]
        d_o, m_o, v_o = refs[4 * n:5 * n], refs[5 * n:6 * n], refs[6 * n:7 * n]
        for i in range(n):
            d, mn, vn = _adamw_math(w_r[i][...], g_r[i][...], m_r[i][...], v_r[i][...])
            d_o[i][...] = d
            m_o[i][...] = mn
            v_o[i][...] = vn

    specs = [pl.BlockSpec(w.shape, lambda i: (0, 0)) for w in ws]
    shapes = [jax.ShapeDtypeStruct(w.shape, F32) for w in ws]
    outs = pl.pallas_call(
        body, name=name, grid=(1,), in_specs=specs * 4, out_specs=specs * 3, out_shape=shapes * 3,
        compiler_params=_params(),
    )(*ws, *gs, *ms, *vs)
    return outs[:n], outs[n:2 * n], outs[2 * n:]


BIG = ("w_in", "w_o_attn", "w_pw_conv", "w_out", "w_ffn_in", "w_ffn_out")
ROW_SHARDED = ("w_out", "w_ffn_out")
SMALL = ("norm1_w", "b_gate", "q_norm_w", "k_norm_w", "conv_w", "conv_b", "conv_ln_w", "conv_ln_b", "norm2_w")
ORDER = ("norm1_w", "w_in", "b_gate", "q_norm_w", "k_norm_w", "w_o_attn", "conv_w", "conv_b", "conv_ln_w",
         "conv_ln_b", "w_pw_conv", "w_out", "norm2_w", "w_ffn_in", "w_ffn_out")
PACK_TILE = 8 * LANES


def _pack_small(parts):
    rows = []
    for p in parts:
        flat = p.reshape(-1)
        pad = (-flat.shape[0]) % PACK_TILE
        rows.append(jnp.pad(flat, (0, pad)).reshape(-1, LANES))
    return jnp.concatenate(rows, axis=0)


def _unpack_small(packed, shapes):
    out, row = [], 0
    for shp in shapes:
        size = int(np.prod(shp))
        nrow = -(-size // PACK_TILE) * (PACK_TILE // LANES)
        out.append(packed[row:row + nrow].reshape(-1)[:size].reshape(shp))
        row += nrow
    return out


def kernel(x, positions, norm1_w, w_in, b_gate, q_norm_w, k_norm_w, w_o_attn, conv_w, conv_b, conv_ln_w, conv_ln_b, w_pw_conv, w_out, norm2_w, w_ffn_in, w_ffn_out, loss_target, m_norm1_w, m_w_in, m_b_gate, m_q_norm_w, m_k_norm_w, m_w_o_attn, m_conv_w, m_conv_b, m_conv_ln_w, m_conv_ln_b, m_w_pw_conv, m_w_out, m_norm2_w, m_w_ffn_in, m_w_ffn_out, v_norm1_w, v_w_in, v_b_gate, v_q_norm_w, v_k_norm_w, v_w_o_attn, v_conv_w, v_conv_b, v_conv_ln_w, v_conv_ln_b, v_w_pw_conv, v_w_out, v_norm2_w, v_w_ffn_in, v_w_ffn_out):
    w = dict(norm1_w=norm1_w, w_in=w_in, b_gate=b_gate, q_norm_w=q_norm_w, k_norm_w=k_norm_w, w_o_attn=w_o_attn,
             conv_w=conv_w, conv_b=conv_b, conv_ln_w=conv_ln_w, conv_ln_b=conv_ln_b, w_pw_conv=w_pw_conv,
             w_out=w_out, norm2_w=norm2_w, w_ffn_in=w_ffn_in, w_ffn_out=w_ffn_out)
    m = dict(norm1_w=m_norm1_w, w_in=m_w_in, b_gate=m_b_gate, q_norm_w=m_q_norm_w, k_norm_w=m_k_norm_w,
             w_o_attn=m_w_o_attn, conv_w=m_conv_w, conv_b=m_conv_b, conv_ln_w=m_conv_ln_w,
             conv_ln_b=m_conv_ln_b, w_pw_conv=m_w_pw_conv, w_out=m_w_out, norm2_w=m_norm2_w,
             w_ffn_in=m_w_ffn_in, w_ffn_out=m_w_ffn_out)
    v = dict(norm1_w=v_norm1_w, w_in=v_w_in, b_gate=v_b_gate, q_norm_w=v_q_norm_w, k_norm_w=v_k_norm_w,
             w_o_attn=v_w_o_attn, conv_w=v_conv_w, conv_b=v_conv_b, conv_ln_w=v_conv_ln_w,
             conv_ln_b=v_conv_ln_b, w_pw_conv=v_w_pw_conv, w_out=v_w_out, norm2_w=v_norm2_w,
             w_ffn_in=v_w_ffn_in, w_ffn_out=v_w_ffn_out)
    cx, cy, cc = _mesh_pos()
    chip = 2 * cx + cy

    chip_arr = chip.reshape(1).astype(jnp.int32)
    pos_arr = jnp.stack([cc, chip]).astype(jnp.int32)
    halves = lambda buf: buf.reshape(N_CHIPS, 2, buf.shape[1] // 2, buf.shape[2])
    w_in_buf = halves(_cast_into_slot(w["w_in"][0], chip_arr, BF16, "cast_w_in"))
    small_bufs = [_cast_into_slot(w[n][0], chip_arr, F32, f"slot_{n}") for n in ("conv_w", "b_gate")]
    first_gather, started = _split_start(_gather_both_legs_rider([w_in_buf], small_bufs, NEAR_CHIPS),
                                         "allgather_w_in_near_start")
    late_bufs = [halves(_cast_into_slot(w[n][0], chip_arr, BF16, f"cast_{n}", after=[started]))
                 for n in LATE_GATHER]
    wts = dict(norm1_w=norm1_w, q_norm_w=q_norm_w, k_norm_w=k_norm_w, conv_b=conv_b, conv_ln_w=conv_ln_w,
               conv_ln_b=conv_ln_b, norm2_w=norm2_w)

    loss, grad_x, g, reduced, w_in_in_flight = _forward_backward(
        x[0], positions.reshape(-1, 1), loss_target[0], wts, first_gather, late_bufs, pos_arr)
    grads = {n: b.reshape(-1, b.shape[2]) for n, b in reduced.items()}

    w_in_in_flight, started = w_in_in_flight
    delta, new_m, new_v = {}, {}, {}
    for n in EARLY_REDUCE:
        grads[n], delta[n], new_m[n], new_v[n] = _adamw(w[n][0], grads[n], m[n][0], v[n][0], f"adamw_{n}",
                                                        after=[started])
    small_parts = [loss] + [g[n] for n in SMALL]
    small_shapes = [p.shape for p in small_parts]
    device_arr = (4 * cx + 2 * cy + cc).reshape(1).astype(jnp.int32)
    small_buf = _cast_into_slot(_pack_small(small_parts), device_arr, F32, "slot_small", n_slots=N_DEVICES)

    (by_chip_w_in,), _ = _split_wait(w_in_in_flight, after=[delta[n] for n in EARLY_REDUCE] + [small_buf],
                                     name="grads_w_in_exchange_wait")
    half_w_in = _sum_chips(by_chip_w_in, pos_arr, "grads_chip_sum_w_in")
    shard_w_in, small_buf = _comm_call(
        _riders_together(_pair_gather_rider([half_w_in]), _small_gather_rider(small_buf)),
        "grads_pair_gather_w_in_small_gather")
    summed = _sum_devices(small_buf, "small_sum")
    reduced = _unpack_small(summed, small_shapes)
    loss_total = reduced[0].reshape(())
    for n, r in zip(SMALL, reduced[1:]):
        grads[n] = r
    ch_shard = conv_w.shape[2]
    grads["conv_w"] = lax.dynamic_slice_in_dim(grads["conv_w"], chip * ch_shard, ch_shard, axis=1)
    d_shard = b_gate.shape[2]
    grads["b_gate"] = lax.dynamic_slice_in_dim(grads["b_gate"], chip * d_shard, d_shard, axis=1)

    grads["w_in"], delta["w_in"], new_m["w_in"], new_v["w_in"] = _adamw(
        w["w_in"][0], shard_w_in.reshape(-1, shard_w_in.shape[2]), m["w_in"][0], v["w_in"][0], "adamw_w_in")
    flat2 = lambda a: a.reshape(-1, a.shape[-1])
    d_s, m_s, v_s = _adamw_small([flat2(w[n]) for n in SMALL], [flat2(grads[n]) for n in SMALL],
                                 [flat2(m[n]) for n in SMALL], [flat2(v[n]) for n in SMALL], "adamw_small")
    for i, n in enumerate(SMALL):
        delta[n], new_m[n], new_v[n] = d_s[i], m_s[i], v_s[i]

    shaped = lambda d, n: d[n].reshape(w[n].shape)
    return (loss_total, grad_x[None], *[shaped(grads, n) for n in ORDER], *[shaped(delta, n) for n in ORDER],
            *[shaped(new_m, n) for n in ORDER], *[shaped(new_v, n) for n in ORDER])
```

```python
import functools

import numpy as np
import jax
import jax.numpy as jnp
from jax import lax
from jax.experimental import pallas as pl
from jax.experimental.pallas import tpu as pltpu

F32 = jnp.float32
BF16 = jnp.bfloat16
MESH = pl.DeviceIdType.MESH
ANY = pl.BlockSpec(memory_space=pl.ANY)

HEAD_DIM = 64
N_SLOT_HEADS = 8
DILATIONS = (1, 4, 16)
HALF_SPAN = 64
ROPE_THETA = 500000.0
ROT_DIM = 16
CONV_WIDTH = 31
EPS = 1e-6
NEG_INF = -1e30
ADAM_LR, ADAM_B1, ADAM_B2, ADAM_EPS, ADAM_WD, ADAM_STEP = 0.001, 0.9, 0.999, 1e-08, 0.01, 10

LANES = 128
QBLK = 128
KWIN = QBLK + 2 * HALF_SPAN
VMEM_LIMIT = 48 * 1024 * 1024
N_CHIPS = 4


def _params(**kw):
    return pltpu.CompilerParams(vmem_limit_bytes=VMEM_LIMIT, **kw)


class _Rider:
    def __init__(self, operands, out_shapes, aliases, scratch, start, wait, middle=None):
        self.operands, self.out_shapes, self.aliases = list(operands), list(out_shapes), dict(aliases)
        self.scratch, self.start, self.wait = list(scratch), start, wait
        self.middle = middle


def _riders_together(a, b):
    n_in, n_out, n_sc = len(a.operands), len(a.out_shapes), len(a.scratch)
    aliases = dict(a.aliases)
    aliases.update({n_in + src: n_out + dst for src, dst in b.aliases.items()})

    def start(r_in, r_out, r_sc):
        a.start(r_in[:n_in], r_out[:n_out], r_sc[:n_sc])
        b.start(r_in[n_in:], r_out[n_out:], r_sc[n_sc:])

    def wait(r_in, r_out, r_sc):
        a.wait(r_in[:n_in], r_out[:n_out], r_sc[:n_sc])
        b.wait(r_in[n_in:], r_out[n_out:], r_sc[n_sc:])

    return _Rider(a.operands + b.operands, a.out_shapes + b.out_shapes, aliases, a.scratch + b.scratch, start, wait)


def _pallas(body, *, name, grid, in_specs, out_specs, out_shape, operands, scratch_shapes=(), aliases=None,
            rider=None, after=()):
    single = not isinstance(out_specs, (list, tuple))
    out_specs_l = [out_specs] if single else list(out_specs)
    out_shape_l = [out_shape] if single else list(out_shape)
    aliases = dict(aliases or {})

    def call(fn, all_in_specs, all_out_specs, all_out_shape, all_scratch, all_aliases, all_operands):
        return pl.pallas_call(
            fn, name=name, grid=grid, in_specs=all_in_specs, out_specs=all_out_specs, out_shape=all_out_shape,
            scratch_shapes=all_scratch, input_output_aliases=all_aliases, compiler_params=_params(),
        )(*all_operands)

    if rider is None:
        n_main = len(in_specs)

        def ordered(*refs):
            body(*refs[:n_main], *refs[n_main + len(after):])

        res = call(ordered if after else body, list(in_specs) + [ANY] * len(after), out_specs_l, out_shape_l,
                   list(scratch_shapes), aliases, list(operands) + list(after))
        return res[0] if single else res
    assert not after
    n_in, n_rin = len(in_specs), len(rider.operands)
    n_out, n_rout = len(out_specs_l), len(rider.out_shapes)
    n_sc = len(scratch_shapes)

    def wrapped(*refs):
        main_in, r_in = refs[:n_in], refs[n_in:n_in + n_rin]
        o0 = n_in + n_rin
        main_out, r_out = refs[o0:o0 + n_out], refs[o0 + n_out:o0 + n_out + n_rout]
        s0 = o0 + n_out + n_rout
        main_sc, r_sc = refs[s0:s0 + n_sc], refs[s0 + n_sc:]
        ids = [pl.program_id(d) for d in range(len(grid))]
        first = functools.reduce(jnp.logical_and, [i == 0 for i in ids])
        last = functools.reduce(jnp.logical_and, [i == n - 1 for i, n in zip(ids, grid)])

        @pl.when(first)
        def _():
            rider.start(r_in, r_out, r_sc)

        body(*main_in, *main_out, *main_sc)

        @pl.when(last)
        def _():
            rider.wait(r_in, r_out, r_sc)

    for src, dst in rider.aliases.items():
        aliases[n_in + src] = n_out + dst
    res = call(wrapped, list(in_specs) + [ANY] * n_rin, out_specs_l + [ANY] * n_rout,
               out_shape_l + rider.out_shapes, list(scratch_shapes) + rider.scratch, aliases,
               list(operands) + rider.operands)
    main = res[:n_out]
    return (main[0] if single else main), res[n_out:]


def _matmul(a, b, *, mode, tm, tn, tk, out_dtype, name, b_blocked=False,
            out_blocked=None, cols_outer=False, rider=None, after=()):
    a_shape = a.shape
    if mode == "nn":
        m_dim, k_dim = a_shape
        n_dim = b.shape[0] * b.shape[2] if b_blocked else b.shape[1]
        rows, cols, red = m_dim, n_dim, k_dim
    elif mode == "nt":
        m_dim, n_dim = a_shape
        k_dim = b.shape[1] if b_blocked else b.shape[0]
        rows, cols, red = m_dim, k_dim, n_dim
    else:
        m_dim, k_dim = a_shape
        n_dim = b.shape[1]
        rows, cols, red = k_dim, n_dim, m_dim
    assert rows % tm == 0 and cols % tn == 0 and red % tk == 0, (name, rows, cols, red)
    ni, nj, nk = rows // tm, cols // tn, red // tk

    if mode == "nn":
        a_spec = pl.BlockSpec((tm, tk), lambda i, j, k: (i, k))
        if b_blocked:
            per = b.shape[2] // tn
            b_spec = pl.BlockSpec((None, tk, tn), lambda i, j, k: (j // per, k, j % per))
        else:
            b_spec = pl.BlockSpec((tk, tn), lambda i, j, k: (k, j))
        dims = (((1,), (0,)), ((), ()))
    elif mode == "nt":
        a_spec = pl.BlockSpec((tm, tk), lambda i, j, k: (i, k))
        if b_blocked:
            per = b.shape[2] // tk
            b_spec = pl.BlockSpec((None, tn, tk), lambda i, j, k: (k // per, j, k % per))
        else:
            b_spec = pl.BlockSpec((tn, tk), lambda i, j, k: (j, k))
        dims = (((1,), (1,)), ((), ()))
    else:
        a_spec = pl.BlockSpec((tk, tm), lambda i, j, k: (k, i))
        b_spec = pl.BlockSpec((tk, tn), lambda i, j, k: (k, j))
        dims = (((0,), (0,)), ((), ()))

    if out_blocked:
        per_o = (cols // out_blocked) // tn
        out_spec = pl.BlockSpec((None, tm, tn), lambda i, j, k: (j // per_o, i, j % per_o))
        out_shape = jax.ShapeDtypeStruct((out_blocked, rows, cols // out_blocked), out_dtype)
    else:
        out_spec = pl.BlockSpec((tm, tn), lambda i, j, k: (i, j))
        out_shape = jax.ShapeDtypeStruct((rows, cols), out_dtype)

    def body(a_ref, b_ref, o_ref, *acc):
        prod = lax.dot_general(a_ref[...], b_ref[...], dims, preferred_element_type=F32)
        if nk == 1:
            o_ref[...] = prod.astype(out_dtype)
        else:
            acc_ref, = acc
            k = pl.program_id(2)

            @pl.when(k == 0)
            def _():
                acc_ref[...] = prod

            @pl.when(k > 0)
            def _():
                acc_ref[...] += prod

            @pl.when(k == nk - 1)
            def _():
                o_ref[...] = acc_ref[...].astype(out_dtype)

    scratch = [pltpu.VMEM((tm, tn), F32)] if nk > 1 else []
    grid = (ni, nj, nk)
    if cols_outer:
        swap = lambda spec: pl.BlockSpec(spec.block_shape, lambda j, i, k, f=spec.index_map: f(i, j, k))
        a_spec, b_spec, out_spec, grid = swap(a_spec), swap(b_spec), swap(out_spec), (nj, ni, nk)
    return _pallas(body, name=name, grid=grid, in_specs=[a_spec, b_spec], out_specs=out_spec,
                   out_shape=out_shape, operands=[a, b], scratch_shapes=scratch, rider=rider, after=after)


def _proj_by_slot(h, w_in, order, first, count, name, proj=None, after=()):
    s, k = h.shape
    nb = w_in.shape[2]
    tm = 512
    prev = [] if proj is None else [proj]

    def body(order_ref, h_ref, w_ref, *refs):
        refs[-1][...] = jnp.dot(h_ref[...], w_ref[...], preferred_element_type=F32)

    return pl.pallas_call(
        body, name=name,
        grid_spec=pltpu.PrefetchScalarGridSpec(
            num_scalar_prefetch=1, grid=(count, s // tm),
            in_specs=[pl.BlockSpec((tm, k), lambda j, i, order_ref: (i, 0)),
                      pl.BlockSpec((None, k, nb), lambda j, i, order_ref: (order_ref[first + j], 0, 0))]
            + [ANY] * (len(prev) + len(after)),
            out_specs=pl.BlockSpec((tm, nb), lambda j, i, order_ref: (i, order_ref[first + j]))),
        out_shape=jax.ShapeDtypeStruct((s, N_CHIPS * nb), F32),
        input_output_aliases={3: 0} if prev else {}, compiler_params=_params(),
    )(order, h, w_in, *prev, *after)


def _rmsnorm_fwd(x, w, name):
    s, d = x.shape
    tm = 256

    def body(x_ref, w_ref, o_ref):
        xv = x_ref[...]
        rstd = lax.rsqrt(jnp.mean(xv * xv, axis=-1, keepdims=True) + EPS)
        o_ref[...] = (xv * rstd * w_ref[...]).astype(BF16)

    return pl.pallas_call(
        body, name=name, grid=(s // tm,),
        in_specs=[pl.BlockSpec((tm, d), lambda i: (i, 0)), pl.BlockSpec((1, d), lambda i: (0, 0))],
        out_specs=pl.BlockSpec((tm, d), lambda i: (i, 0)),
        out_shape=jax.ShapeDtypeStruct((s, d), BF16), compiler_params=_params(),
    )(x, w)


def _matmul_nt_rmsnorm_bwd(dz, w_blocked, x, w, dres, name, rider=None, after=()):
    s, d = x.shape
    nk, _, nb = w_blocked.shape
    tm = 512
    nt_dims = (((1,), (1,)), ((), ()))

    def body(a_ref, b_ref, x_ref, w_ref, dres_ref, dx_ref, dxb_ref, dw_ref, acc_ref):
        i, k = pl.program_id(0), pl.program_id(1)
        prod = lax.dot_general(a_ref[...], b_ref[...], nt_dims, preferred_element_type=F32)

        @pl.when(k == 0)
        def _():
            acc_ref[...] = prod

        @pl.when(k > 0)
        def _():
            acc_ref[...] += prod

        @pl.when(k == nk - 1)
        def _():
            xv = x_ref[...]
            rstd = lax.rsqrt(jnp.mean(xv * xv, axis=-1, keepdims=True) + EPS)
            xhat = xv * rstd
            dhv = acc_ref[...]
            g = dhv * w_ref[...]
            dx = rstd * (g - xhat * jnp.mean(g * xhat, axis=-1, keepdims=True)) + dres_ref[...]
            dx_ref[...] = dx
            dxb_ref[...] = dx.astype(BF16)
            part = jnp.sum(dhv * xhat, axis=0, keepdims=True)

            @pl.when(i == 0)
            def _():
                dw_ref[...] = part

            @pl.when(i > 0)
            def _():
                dw_ref[...] += part

    row = pl.BlockSpec((tm, d), lambda i, k: (i, 0))
    vec = pl.BlockSpec((1, d), lambda i, k: (0, 0))
    return _pallas(
        body, name=name, grid=(s // tm, nk),
        in_specs=[pl.BlockSpec((tm, nb), lambda i, k: (i, k)), pl.BlockSpec((None, d, nb), lambda i, k: (k, 0, 0)),
                  row, vec, row],
        out_specs=[row, row, vec],
        out_shape=[jax.ShapeDtypeStruct((s, d), F32), jax.ShapeDtypeStruct((s, d), BF16),
                   jax.ShapeDtypeStruct((1, d), F32)],
        operands=[dz, w_blocked, x, w, dres], scratch_shapes=[pltpu.VMEM((tm, d), F32)], rider=rider, after=after)


def _rope_consts():
    lane = np.arange(LANES)
    in_head = lane % HEAD_DIM
    inv_freq = ROPE_THETA ** (-jnp.arange(0, ROT_DIM, 2, dtype=F32) / ROT_DIM)
    invf = jnp.where(jnp.asarray(in_head < ROT_DIM), jnp.tile(inv_freq, LANES // (ROT_DIM // 2)), 0.0)
    m_a = np.where(in_head < ROT_DIM // 2, -1.0, 0.0).astype(np.float32)
    m_b = np.where((in_head >= ROT_DIM // 2) & (in_head < ROT_DIM), 1.0, 0.0).astype(np.float32)
    block_diag = (lane[:, None] // HEAD_DIM == lane[None, :] // HEAD_DIM).astype(np.float32)
    return (invf.reshape(1, LANES).astype(F32), jnp.asarray(m_a).reshape(1, LANES),
            jnp.asarray(m_b).reshape(1, LANES), jnp.asarray(block_diag, dtype=BF16))


def _head_sums(v, bd):
    hi = v.astype(BF16)
    lo = (v - hi.astype(F32)).astype(BF16)
    return jnp.dot(hi, bd, preferred_element_type=F32) + jnp.dot(lo, bd, preferred_element_type=F32)


def _qk_fwd(proj, pos_col, qw2, kw2, consts, name, rider=None):
    s = proj.shape[0]
    width = 3 * N_SLOT_HEADS * HEAD_DIM
    tm = 128
    invf, m_a, m_b, bd = consts
    scale = HEAD_DIM ** -0.5

    def body(q_ref, k_ref, pos_ref, qw_ref, kw_ref, invf_ref, ma_ref, mb_ref, bd_ref, qo_ref, ko_ref):
        ang = pos_ref[...].astype(F32) * invf_ref[...]
        cos = jnp.cos(ang)
        sin = jnp.sin(ang)
        s_a = sin * ma_ref[...]
        s_b = sin * mb_ref[...]
        bdv = bd_ref[...]
        for src, w_ref, dst, sc in ((q_ref, qw_ref, qo_ref, scale), (k_ref, kw_ref, ko_ref, 1.0)):
            for cb in range(width // LANES):
                cols = slice(cb * LANES, (cb + 1) * LANES)
                t = src[:, cols]
                rstd = lax.rsqrt(_head_sums(t * t, bdv) * (1.0 / HEAD_DIM) + EPS)
                y = t * rstd * w_ref[...]
                r = y * cos + pltpu.roll(y, LANES - 8, axis=1) * s_a + pltpu.roll(y, 8, axis=1) * s_b
                dst[:, cols] = r * sc if sc != 1.0 else r

    vec = pl.BlockSpec((1, LANES), lambda i: (0, 0))
    return _pallas(
        body, name=name, grid=(s // tm,),
        in_specs=[pl.BlockSpec((tm, width), lambda i: (i, 0)), pl.BlockSpec((tm, width), lambda i: (i, 1)),
                  pl.BlockSpec((tm, 1), lambda i: (i, 0)), vec, vec, vec, vec, vec,
                  pl.BlockSpec((LANES, LANES), lambda i: (0, 0))],
        out_specs=[pl.BlockSpec((tm, width), lambda i: (i, 0))] * 2,
        out_shape=[jax.ShapeDtypeStruct((s, width), F32)] * 2,
        operands=[proj, proj, pos_col, qw2, kw2, invf, m_a, m_b, bd], rider=rider)


def _qk_bwd(dqn, dkn, dv, da, db, dgl, proj, pos_col, qw2, kw2, consts, name, rider=None):
    s = proj.shape[0]
    width = 3 * N_SLOT_HEADS * HEAD_DIM
    ch = da.shape[1]
    gate_w = dgl.shape[1]
    out_w = 3 * width + 2 * ch + gate_w
    assert out_w == proj.shape[1]
    tm = 128
    invf, m_a, m_b, bd = consts
    scale = HEAD_DIM ** -0.5

    def body(dq_ref, dk_ref, dv_ref, da_ref, db_ref, dgl_ref, q_ref, k_ref, pos_ref, qw_ref, kw_ref,
             invf_ref, ma_ref, mb_ref, bd_ref, out_ref, dqw_ref, dkw_ref):
        ang = pos_ref[...].astype(F32) * invf_ref[...]
        cos = jnp.cos(ang)
        sin = jnp.sin(ang)
        s_a = sin * ma_ref[...]
        s_b = sin * mb_ref[...]
        bdv = bd_ref[...]
        first = pl.program_id(0) == 0
        for src, dsrc, w_ref, col0, dw_ref, sc in ((q_ref, dq_ref, qw_ref, 0, dqw_ref, scale),
                                                   (k_ref, dk_ref, kw_ref, width, dkw_ref, 1.0)):
            dw_acc = jnp.zeros((1, LANES), F32)
            for cb in range(width // LANES):
                cols = slice(cb * LANES, (cb + 1) * LANES)
                t = src[:, cols]
                dr = dsrc[:, cols]
                if sc != 1.0:
                    dr = dr * sc
                dy = dr * cos + pltpu.roll(dr * s_a, 8, axis=1) + pltpu.roll(dr * s_b, LANES - 8, axis=1)
                rstd = lax.rsqrt(_head_sums(t * t, bdv) * (1.0 / HEAD_DIM) + EPS)
                xhat = t * rstd
                g = dy * w_ref[...]
                dt = rstd * (g - xhat * (_head_sums(g * xhat, bdv) * (1.0 / HEAD_DIM)))
                out_ref[:, col0 + cb * LANES: col0 + (cb + 1) * LANES] = dt.astype(BF16)
                dw_acc = dw_acc + jnp.sum(dy * xhat, axis=0, keepdims=True)
            dw_acc = dw_acc + pltpu.roll(dw_acc, HEAD_DIM, axis=1)

            @pl.when(first)
            def _(dw_ref=dw_ref, dw_acc=dw_acc):
                dw_ref[...] = dw_acc

            @pl.when(jnp.logical_not(first))
            def _(dw_ref=dw_ref, dw_acc=dw_acc):
                dw_ref[...] += dw_acc
        out_ref[:, 2 * width: 3 * width] = dv_ref[...].astype(BF16)
        out_ref[:, 3 * width: 3 * width + ch] = da_ref[...]
        out_ref[:, 3 * width + ch: 3 * width + 2 * ch] = db_ref[...]
        out_ref[:, 3 * width + 2 * ch: out_w] = dgl_ref[...]

    vec = pl.BlockSpec((1, LANES), lambda i: (0, 0))
    blk = lambda c: pl.BlockSpec((tm, width), lambda i: (i, c))
    cblk = pl.BlockSpec((tm, ch), lambda i: (i, 0))
    return _pallas(
        body, name=name, grid=(s // tm,),
        in_specs=[blk(0), blk(0), blk(0), cblk, cblk, pl.BlockSpec((tm, gate_w), lambda i: (i, 0)),
                  blk(0), blk(1), pl.BlockSpec((tm, 1), lambda i: (i, 0)), vec, vec, vec, vec, vec,
                  pl.BlockSpec((LANES, LANES), lambda i: (0, 0))],
        out_specs=[pl.BlockSpec((tm, out_w), lambda i: (i, 0)), vec, vec],
        out_shape=[jax.ShapeDtypeStruct((s, out_w), BF16)] + [jax.ShapeDtypeStruct((1, LANES), F32)] * 2,
        operands=[dqn, dkn, dv, da, db, dgl, proj, proj, pos_col, qw2, kw2, invf, m_a, m_b, bd],
        rider=rider)


def _row_chunks(n_rows, fn, chunk=256):
    def step(i, c):
        fn(pl.ds(pl.multiple_of(i * chunk, chunk), chunk))
        return c
    lax.fori_loop(0, n_rows // chunk, step, 0)


def _to_residue_major(dst, src, s, d, dst_off=0, cast=None):
    seq = s // d
    for r in range(d):
        v = src[...] if d == 1 else src[pl.ds(r, seq, stride=d), :]
        dst[dst_off + r * seq: dst_off + (r + 1) * seq, :] = v if cast is None else v.astype(cast)


def _from_residue_major(dst, src, s, d, src_off=0):
    seq = s // d
    for r in range(d):
        v = src[src_off + r * seq: src_off + (r + 1) * seq, :]
        if d == 1:
            dst[...] = v
        else:
            dst[pl.ds(r, seq, stride=d), :] = v


def _band_bias():
    qi = lax.broadcasted_iota(jnp.int32, (QBLK, KWIN), 0)
    kj = lax.broadcasted_iota(jnp.int32, (QBLK, KWIN), 1)
    return jnp.where(jnp.abs(kj - HALF_SPAN - qi) <= HALF_SPAN, 0.0, NEG_INF).astype(F32)


def _range_bias(base, seq):
    kj = lax.broadcasted_iota(jnp.int32, (1, KWIN), 1)
    lo = (base & -seq) - base + HALF_SPAN
    return jnp.where((kj >= lo) & (kj < lo + seq), 0.0, NEG_INF).astype(F32)


def _skewed_blocks(n_blk, produce, consume):
    produce(0, 0)
    for b in range(n_blk):
        consume(b, b % 2)
        if b + 1 < n_blk:
            produce(b + 1, (b + 1) % 2)


def _block_base(b):
    return b * QBLK if isinstance(b, int) else pl.multiple_of(b * QBLK, QBLK)


def _attn_fwd(qn, kn, proj, name, rider=None):
    s = qn.shape[0]
    n_pairs = N_SLOT_HEADS * HEAD_DIM // LANES
    v_col0 = 2 * qn.shape[1] // LANES
    nt_dims = (((1,), (1,)), ((), ()))

    def body(q_ref, k_ref, v_ref, attn_ref, lse_ref, attn_b_ref, q_rm, k_rm, v_rm, acc_rm, m_rm, l_rm,
             acc_p, m_p, l_p, m_run, l_run, acc_run, band, s_buf, m_buf):
        g = pl.program_id(1)
        zpad = jnp.zeros((HALF_SPAN, LANES), BF16)
        k_rm[0:HALF_SPAN, :] = zpad
        k_rm[s + HALF_SPAN: s + 2 * HALF_SPAN, :] = zpad
        v_rm[0:HALF_SPAN, 0:LANES] = zpad
        v_rm[s + HALF_SPAN: s + 2 * HALF_SPAN, 0:LANES] = zpad

        def ones_rows(rows):
            v_rm[pl.ds(rows.start, rows.size), LANES:2 * LANES] = jnp.ones((rows.size, LANES), BF16)

        _row_chunks(s + 2 * HALF_SPAN, ones_rows, chunk=2 * HALF_SPAN)
        band[...] = _band_bias()
        lane = lax.broadcasted_iota(jnp.int32, (QBLK, LANES), 1)
        low = lane < HEAD_DIM
        n_blk = s // QBLK

        for gi, d in enumerate(DILATIONS):
            @pl.when(g == gi)
            def _(gi=gi, d=d):
                seq = s // d
                _to_residue_major(q_rm, q_ref, s, d, cast=BF16)
                _to_residue_major(k_rm, k_ref, s, d, dst_off=HALF_SPAN, cast=BF16)
                _to_residue_major(v_rm.at[:, 0:LANES], v_ref, s, d, dst_off=HALF_SPAN, cast=BF16)

                def scores(b, slot):
                    base = _block_base(b)
                    q = q_rm[pl.ds(base, QBLK), :]
                    zero = jnp.zeros_like(q)
                    q2 = jnp.concatenate([jnp.where(low, q, zero), jnp.where(low, zero, q)], axis=0)
                    sc = lax.dot_general(q2, k_rm[pl.ds(base, KWIN), :], nt_dims, preferred_element_type=F32)
                    bias = band[...] + _range_bias(base, seq)
                    for hh in range(2):
                        rows = slice(hh * QBLK, (hh + 1) * QBLK)
                        sh = sc[rows, :] + bias
                        s_buf[slot, rows, :] = sh
                        m_buf[slot, rows, :] = jnp.broadcast_to(jnp.max(sh, axis=-1, keepdims=True), (QBLK, LANES))

                def outputs(b, slot):
                    base = _block_base(b)
                    sv = s_buf[slot]
                    mb = m_buf[slot]
                    p = jnp.exp(jnp.concatenate([sv[:, 0:LANES] - mb, sv[:, LANES:2 * LANES] - mb], axis=1))
                    pv = jnp.dot(p.astype(BF16), v_rm[pl.ds(base, KWIN), :], preferred_element_type=F32)
                    rows = pl.ds(base, QBLK)
                    acc_rm[rows, :] = jnp.where(low, pv[0:QBLK, 0:LANES], pv[QBLK:2 * QBLK, 0:LANES])
                    l_rm[rows, :] = jnp.where(low, pv[0:QBLK, LANES:2 * LANES], pv[QBLK:2 * QBLK, LANES:2 * LANES])
                    m_rm[rows, :] = jnp.where(low, mb[0:QBLK, :], mb[QBLK:2 * QBLK, :])

                _skewed_blocks(n_blk, scores, outputs)
                if d == 1:
                    src = (acc_rm, m_rm, l_rm)
                else:
                    for dst_, src_ in ((acc_p, acc_rm), (m_p, m_rm), (l_p, l_rm)):
                        _from_residue_major(dst_, src_, s, d)
                    src = (acc_p, m_p, l_p)

                def combine(rows):
                    a_g, m_g, l_g = src[0][rows, :], src[1][rows, :], src[2][rows, :]
                    if gi == 0:
                        m_new, l_new, a_new = m_g, l_g, a_g
                    else:
                        m_old = m_run[rows, :]
                        m_new = jnp.maximum(m_old, m_g)
                        w_old = jnp.exp(m_old - m_new)
                        w_g = jnp.exp(m_g - m_new)
                        l_new = l_run[rows, :] * w_old + l_g * w_g
                        a_new = acc_run[rows, :] * w_old + a_g * w_g
                    if gi == len(DILATIONS) - 1:
                        out = a_new / l_new
                        attn_ref[rows, :] = out
                        attn_b_ref[rows, :] = out.astype(BF16)
                        lse_ref[rows, :] = m_new + jnp.log(l_new)
                    else:
                        m_run[rows, :] = m_new
                        l_run[rows, :] = l_new
                        acc_run[rows, :] = a_new

                _row_chunks(s, combine)

    qk_spec = pl.BlockSpec((s, LANES), lambda hp, g: (0, g * n_pairs + hp))
    v_spec = pl.BlockSpec((s, LANES), lambda hp, g: (0, v_col0 + g * n_pairs + hp))
    o_spec = pl.BlockSpec((s, LANES), lambda hp, g: (0, hp))
    f32buf = pltpu.VMEM((s, LANES), F32)
    return _pallas(
        body, name=name, grid=(n_pairs, len(DILATIONS)), in_specs=[qk_spec, qk_spec, v_spec],
        out_specs=[o_spec, o_spec, o_spec],
        out_shape=[jax.ShapeDtypeStruct((s, n_pairs * LANES), F32)] * 2
        + [jax.ShapeDtypeStruct((s, n_pairs * LANES), BF16)],
        operands=[qn, kn, proj],
        scratch_shapes=[pltpu.VMEM((s, LANES), BF16), pltpu.VMEM((s + 2 * HALF_SPAN, LANES), BF16),
                        pltpu.VMEM((s + 2 * HALF_SPAN, 2 * LANES), BF16)] + [f32buf] * 9
        + [pltpu.VMEM((QBLK, KWIN), F32), pltpu.VMEM((2, 2 * QBLK, KWIN), F32),
           pltpu.VMEM((2, 2 * QBLK, LANES), F32)],
        rider=rider)


def _attn_bwd(qn, kn, proj, dattn, attn, lse, bd, name, rider=None):
    s = qn.shape[0]
    n_pairs = N_SLOT_HEADS * HEAD_DIM // LANES
    v_col0 = 2 * qn.shape[1] // LANES
    nt_dims = (((1,), (1,)), ((), ()))
    tn_dims = (((0,), (0,)), ((), ()))
    spad = s + 2 * HALF_SPAN

    def body(q_ref, k_ref, v_ref, do_ref, o_ref, lse_ref, bd_ref, dq_ref, dk_ref, dv_ref,
             q_rm, k_rm, v_rm, do_rm, lse0_rm, lse1_rm, dd0_rm, dd1_rm, dq_rm, dk_rm, dv_rm,
             lse0_p, lse1_p, dd0_p, dd1_p, band, p_buf, ds_buf):
        g = pl.program_id(1)
        zpad = jnp.zeros((HALF_SPAN, LANES), BF16)
        for buf in (k_rm, v_rm):
            buf[0:HALF_SPAN, :] = zpad
            buf[s + HALF_SPAN: spad, :] = zpad
        zf = jnp.zeros((HALF_SPAN, LANES), F32)
        for buf in (dk_rm, dv_rm):
            buf[0:HALF_SPAN, :] = zf
            buf[s + HALF_SPAN: spad, :] = zf
        band[...] = _band_bias()

        def clear(rows):
            z = jnp.zeros((rows.size, LANES), F32)
            dk_rm[pl.ds(rows.start + HALF_SPAN, rows.size), :] = z
            dv_rm[pl.ds(rows.start + HALF_SPAN, rows.size), :] = z

        _row_chunks(s, clear)

        def prepare(rows):
            lo = lax.broadcasted_iota(jnp.int32, (rows.size, LANES), 1) < HEAD_DIM
            dsum = _head_sums(do_ref[rows, :] * o_ref[rows, :], bd_ref[...])
            dswap = pltpu.roll(dsum, HEAD_DIM, axis=1)
            dd0_p[rows, :] = jnp.where(lo, dsum, dswap)
            dd1_p[rows, :] = jnp.where(lo, dswap, dsum)
            lv = lse_ref[rows, :]
            lswap = pltpu.roll(lv, HEAD_DIM, axis=1)
            lse0_p[rows, :] = jnp.where(lo, lv, lswap)
            lse1_p[rows, :] = jnp.where(lo, lswap, lv)

        @pl.when(g == 0)
        def _():
            _row_chunks(s, prepare)
        lane = lax.broadcasted_iota(jnp.int32, (QBLK, LANES), 1)
        low = lane < HEAD_DIM
        n_blk = s // QBLK

        def stacked(ref, rows):
            val = ref[rows, :]
            zero = jnp.zeros_like(val)
            return jnp.concatenate([jnp.where(low, val, zero), jnp.where(low, zero, val)], axis=0)

        for gi, d in enumerate(DILATIONS):
            @pl.when(g == gi)
            def _(d=d):
                seq = s // d
                _to_residue_major(q_rm, q_ref, s, d, cast=BF16)
                _to_residue_major(k_rm, k_ref, s, d, dst_off=HALF_SPAN, cast=BF16)
                _to_residue_major(v_rm, v_ref, s, d, dst_off=HALF_SPAN, cast=BF16)
                _to_residue_major(do_rm, do_ref, s, d, cast=BF16)
                for dst_, src_ in ((lse0_rm, lse0_p), (lse1_rm, lse1_p), (dd0_rm, dd0_p), (dd1_rm, dd1_p)):
                    _to_residue_major(dst_, src_, s, d)

                def scores(b, slot):
                    base = _block_base(b)
                    rows = pl.ds(base, QBLK)
                    win = pl.ds(base, KWIN)
                    sc = lax.dot_general(stacked(q_rm, rows), k_rm[win, :], nt_dims, preferred_element_type=F32)
                    dp = lax.dot_general(stacked(do_rm, rows), v_rm[win, :], nt_dims, preferred_element_type=F32)
                    bias = band[...] + _range_bias(base, seq)
                    for hh, (lse_r, dd_r) in enumerate(((lse0_rm, dd0_rm), (lse1_rm, dd1_rm))):
                        r = slice(hh * QBLK, (hh + 1) * QBLK)
                        lse_h = lse_r[rows, :]
                        dd_h = dd_r[rows, :]
                        sh = sc[r, :] + bias
                        p = jnp.exp(jnp.concatenate([sh[:, 0:LANES] - lse_h, sh[:, LANES:KWIN] - lse_h], axis=1))
                        dph = dp[r, :]
                        ds = p * jnp.concatenate([dph[:, 0:LANES] - dd_h, dph[:, LANES:KWIN] - dd_h], axis=1)
                        p_buf[slot, r, :] = p.astype(BF16)
                        ds_buf[slot, r, :] = ds.astype(BF16)

                def grads(b, slot):
                    base = _block_base(b)
                    rows = pl.ds(base, QBLK)
                    win = pl.ds(base, KWIN)
                    p = p_buf[slot]
                    ds = ds_buf[slot]
                    dq2 = jnp.dot(ds, k_rm[win, :], preferred_element_type=F32)
                    dq_rm[rows, :] = jnp.where(low, dq2[0:QBLK, :], dq2[QBLK:2 * QBLK, :])
                    dk_rm[win, :] += lax.dot_general(ds, stacked(q_rm, rows), tn_dims, preferred_element_type=F32)
                    dv_rm[win, :] += lax.dot_general(p, stacked(do_rm, rows), tn_dims, preferred_element_type=F32)

                _skewed_blocks(n_blk, scores, grads)
                _from_residue_major(dq_ref, dq_rm, s, d)
                _from_residue_major(dk_ref, dk_rm, s, d, src_off=HALF_SPAN)
                _from_residue_major(dv_ref, dv_rm, s, d, src_off=HALF_SPAN)

    qk_spec = pl.BlockSpec((s, LANES), lambda hp, g: (0, g * n_pairs + hp))
    v_spec = pl.BlockSpec((s, LANES), lambda hp, g: (0, v_col0 + g * n_pairs + hp))
    o_spec = pl.BlockSpec((s, LANES), lambda hp, g: (0, hp))
    width = qn.shape[1]
    f32buf = pltpu.VMEM((s, LANES), F32)
    f32pad = pltpu.VMEM((spad, LANES), F32)
    return _pallas(
        body, name=name, grid=(n_pairs, len(DILATIONS)),
        in_specs=[qk_spec, qk_spec, v_spec, o_spec, o_spec, o_spec,
                  pl.BlockSpec((LANES, LANES), lambda hp, g: (0, 0))],
        out_specs=[qk_spec, qk_spec, qk_spec],
        out_shape=[jax.ShapeDtypeStruct((s, width), F32)] * 3,
        operands=[qn, kn, proj, dattn, attn, lse, bd],
        scratch_shapes=[pltpu.VMEM((s, LANES), BF16), pltpu.VMEM((spad, LANES), BF16),
                        pltpu.VMEM((spad, LANES), BF16), pltpu.VMEM((s, LANES), BF16),
                        f32buf, f32buf, f32buf, f32buf, f32buf, f32pad, f32pad,
                        f32buf, f32buf, f32buf, f32buf, pltpu.VMEM((QBLK, KWIN), F32),
                        pltpu.VMEM((2, 2 * QBLK, KWIN), BF16), pltpu.VMEM((2, 2 * QBLK, KWIN), BF16)],
        rider=rider)


CONV_PAD = 16


def _conv_fwd(proj, conv_w, conv_b, col0, name, rider=None):
    s = proj.shape[0]
    ch = conv_w.shape[1]
    nblk = ch // LANES
    a0 = col0 // LANES
    tr = 256
    shift = CONV_PAD - (CONV_WIDTH - 1) // 2

    def body(a_ref, b_ref, w_ref, bias_ref, u0_ref, uc_ref, pad):
        z = jnp.zeros((CONV_PAD, LANES), F32)
        pad[0:CONV_PAD, :] = z
        pad[s + CONV_PAD: s + 2 * CONV_PAD, :] = z

        def glu(rows):
            u0 = a_ref[rows, :] * jax.nn.sigmoid(b_ref[rows, :])
            u0_ref[rows, :] = u0
            pad[pl.ds(rows.start + CONV_PAD, rows.size), :] = u0

        _row_chunks(s, glu)
        for t in range(0, s, tr):
            acc = jnp.broadcast_to(bias_ref[...], (tr, LANES))
            for k in range(CONV_WIDTH):
                acc = acc + w_ref[k:k + 1, :] * pad[t + k + shift: t + k + shift + tr, :]
            uc_ref[t:t + tr, :] = acc

    return _pallas(
        body, name=name, grid=(nblk,),
        in_specs=[pl.BlockSpec((s, LANES), lambda c: (0, a0 + c)),
                  pl.BlockSpec((s, LANES), lambda c: (0, a0 + nblk + c)),
                  pl.BlockSpec((CONV_WIDTH, LANES), lambda c: (0, c)),
                  pl.BlockSpec((1, LANES), lambda c: (0, c))],
        out_specs=[pl.BlockSpec((s, LANES), lambda c: (0, c))] * 2,
        out_shape=[jax.ShapeDtypeStruct((s, ch), F32)] * 2, operands=[proj, proj, conv_w, conv_b],
        scratch_shapes=[pltpu.VMEM((s + 2 * CONV_PAD, LANES), F32)], rider=rider)


def _ln_silu_fwd(uc, ln_w, ln_b, name):
    s, ch = uc.shape
    tm = 256

    def body(u_ref, w_ref, b_ref, o_ref):
        u = u_ref[...]
        mu = jnp.mean(u, axis=-1, keepdims=True)
        xc = u - mu
        rstd = lax.rsqrt(jnp.mean(xc * xc, axis=-1, keepdims=True) + EPS)
        z = xc * rstd * w_ref[...] + b_ref[...]
        o_ref[...] = (z * jax.nn.sigmoid(z)).astype(BF16)

    row = pl.BlockSpec((tm, ch), lambda i: (i, 0))
    vec = pl.BlockSpec((1, ch), lambda i: (0, 0))
    return pl.pallas_call(
        body, name=name, grid=(s // tm,), in_specs=[row, vec, vec], out_specs=row,
        out_shape=jax.ShapeDtypeStruct((s, ch), BF16), compiler_params=_params(),
    )(uc, ln_w, ln_b)


def _ln_silu_bwd(du3, uc, ln_w, ln_b, name):
    s, ch = uc.shape
    tm = 256

    def body(d_ref, u_ref, w_ref, b_ref, du_ref, dw_ref, db_ref):
        u = u_ref[...]
        mu = jnp.mean(u, axis=-1, keepdims=True)
        xc = u - mu
        rstd = lax.rsqrt(jnp.mean(xc * xc, axis=-1, keepdims=True) + EPS)
        xhat = xc * rstd
        z = xhat * w_ref[...] + b_ref[...]
        sg = jax.nn.sigmoid(z)
        dz = d_ref[...] * (sg * (1.0 + z * (1.0 - sg)))
        dxh = dz * w_ref[...]
        du_ref[...] = rstd * (dxh - jnp.mean(dxh, axis=-1, keepdims=True)
                              - xhat * jnp.mean(dxh * xhat, axis=-1, keepdims=True))
        pw = jnp.sum(dz * xhat, axis=0, keepdims=True)
        pb = jnp.sum(dz, axis=0, keepdims=True)
        first = pl.program_id(0) == 0

        @pl.when(first)
        def _():
            dw_ref[...] = pw
            db_ref[...] = pb

        @pl.when(jnp.logical_not(first))
        def _():
            dw_ref[...] += pw
            db_ref[...] += pb

    row = pl.BlockSpec((tm, ch), lambda i: (i, 0))
    vec = pl.BlockSpec((1, ch), lambda i: (0, 0))
    return pl.pallas_call(
        body, name=name, grid=(s // tm,), in_specs=[row, row, vec, vec], out_specs=[row, vec, vec],
        out_shape=[jax.ShapeDtypeStruct((s, ch), F32), jax.ShapeDtypeStruct((1, ch), F32),
                   jax.ShapeDtypeStruct((1, ch), F32)],
        compiler_params=_params(),
    )(du3, uc, ln_w, ln_b)


def _conv_bwd(duc, u0, proj, conv_w, col0, name, rider=None):
    s = proj.shape[0]
    ch = conv_w.shape[1]
    nblk = ch // LANES
    a0 = col0 // LANES
    tr = 256
    half = (CONV_WIDTH - 1) // 2
    shift = CONV_PAD - half

    def body(duc_ref, u0_ref, a_ref, b_ref, w_ref, da_ref, db_ref, dw_ref, dbias_ref, pad_d, pad_u):
        z = jnp.zeros((CONV_PAD, LANES), F32)
        for buf in (pad_d, pad_u):
            buf[0:CONV_PAD, :] = z
            buf[s + CONV_PAD: s + 2 * CONV_PAD, :] = z

        def fill(rows):
            dst = pl.ds(rows.start + CONV_PAD, rows.size)
            pad_d[dst, :] = duc_ref[rows, :]
            pad_u[dst, :] = u0_ref[rows, :]

        _row_chunks(s, fill)
        dw_acc = [jnp.zeros((8, LANES), F32) for _ in range(CONV_WIDTH)]
        dbias_acc = jnp.zeros((8, LANES), F32)
        for t in range(0, s, tr):
            d_t = duc_ref[t:t + tr, :]
            dbias_acc = dbias_acc + jnp.sum(d_t.reshape(tr // 8, 8, LANES), axis=0)
            du0 = jnp.zeros((tr, LANES), F32)
            for k in range(CONV_WIDTH):
                du0 = du0 + w_ref[k:k + 1, :] * pad_d[t - k + half + CONV_PAD: t - k + half + CONV_PAD + tr, :]
                prod = d_t * pad_u[t + k + shift: t + k + shift + tr, :]
                dw_acc[k] = dw_acc[k] + jnp.sum(prod.reshape(tr // 8, 8, LANES), axis=0)
            av = a_ref[t:t + tr, :]
            sg = jax.nn.sigmoid(b_ref[t:t + tr, :])
            da_ref[t:t + tr, :] = (du0 * sg).astype(BF16)
            db_ref[t:t + tr, :] = (du0 * av * sg * (1.0 - sg)).astype(BF16)
        for k in range(CONV_WIDTH):
            dw_ref[k:k + 1, :] = jnp.sum(dw_acc[k], axis=0, keepdims=True)
        dbias_ref[...] = jnp.sum(dbias_acc, axis=0, keepdims=True)

    col = lambda off: pl.BlockSpec((s, LANES), lambda c: (0, off + c))
    return _pallas(
        body, name=name, grid=(nblk,),
        in_specs=[col(0), col(0), col(a0), col(a0 + nblk),
                  pl.BlockSpec((CONV_WIDTH, LANES), lambda c: (0, c))],
        out_specs=[col(0), col(0), pl.BlockSpec((CONV_WIDTH, LANES), lambda c: (0, c)),
                   pl.BlockSpec((1, LANES), lambda c: (0, c))],
        out_shape=[jax.ShapeDtypeStruct((s, ch), BF16)] * 2
        + [jax.ShapeDtypeStruct((CONV_WIDTH, ch), F32), jax.ShapeDtypeStruct((1, ch), F32)],
        operands=[duc, u0, proj, proj, conv_w],
        scratch_shapes=[pltpu.VMEM((s + 2 * CONV_PAD, LANES), F32)] * 2, rider=rider)


def _mix_out_proj(attn_b, u3, w_o, w_pw, proj, bg, col0, w_out, x, norm_w, name, rider=None):
    s, d = x.shape
    k = attn_b.shape[1]
    tm = 256
    half = d // 2
    assert col0 % half == 0
    c0 = col0 // half

    def body(a_ref, u_ref, wo_ref, wp_ref, a0_ref, a1_ref, b0_ref, b1_ref, bias_ref, w_ref, x_ref, nw_ref,
             ya_ref, yb_ref, mixed_ref, x1_ref, h2_ref):
        mixed = None
        for br, (src_ref, wb_ref, lo_ref, hi_ref, y_ref) in enumerate(((a_ref, wo_ref, a0_ref, a1_ref, ya_ref),
                                                                       (u_ref, wp_ref, b0_ref, b1_ref, yb_ref))):
            src = src_ref[...]
            y = jnp.concatenate([jnp.dot(src, wb_ref[j], preferred_element_type=F32) for j in range(N_CHIPS)],
                                axis=1)
            y_ref[...] = y
            logits = jnp.concatenate([lo_ref[...], hi_ref[...]], axis=1)
            part = jax.nn.sigmoid(logits + bias_ref[br]) * y
            mixed = part if mixed is None else mixed + part
        mixed = mixed.astype(BF16)
        mixed_ref[...] = mixed
        x1 = x_ref[...] + jnp.dot(mixed, w_ref[...], preferred_element_type=F32)
        x1_ref[...] = x1
        rstd = lax.rsqrt(jnp.mean(x1 * x1, axis=-1, keepdims=True) + EPS)
        h2_ref[...] = (x1 * rstd * nw_ref[...]).astype(BF16)

    row = pl.BlockSpec((tm, d), lambda i: (i, 0))
    src_row = pl.BlockSpec((tm, k), lambda i: (i, 0))
    blocks = pl.BlockSpec(w_o.shape, lambda i: (0, 0, 0))
    logit_blk = lambda j: pl.BlockSpec((tm, half), functools.partial(lambda i, j: (i, c0 + j), j=j))
    return _pallas(
        body, name=name, grid=(s // tm,),
        in_specs=[src_row, src_row, blocks, blocks, logit_blk(0), logit_blk(1), logit_blk(2), logit_blk(3),
                  pl.BlockSpec((2, 1, d), lambda i: (0, 0, 0)), pl.BlockSpec((d, d), lambda i: (0, 0)), row,
                  pl.BlockSpec((1, d), lambda i: (0, 0))],
        out_specs=[row] * 5,
        out_shape=[jax.ShapeDtypeStruct((s, d), F32), jax.ShapeDtypeStruct((s, d), F32),
                   jax.ShapeDtypeStruct((s, d), BF16), jax.ShapeDtypeStruct((s, d), F32),
                   jax.ShapeDtypeStruct((s, d), BF16)],
        operands=[attn_b, u3, w_o, w_pw, proj, proj, proj, proj, bg, w_out, x, norm_w], rider=rider)


def _out_proj_bwd_gates(dx1, w_out, proj, bg, y_a, y_b, w_o, w_pw, col0, name, after=()):
    s, d = y_a.shape
    n_blk, k, blk = w_o.shape
    tm = 256
    half = d // 2
    assert col0 % half == 0
    c0 = col0 // half
    nt_dims = (((1,), (1,)), ((), ()))

    def body(dx_ref, w_ref, a0_ref, a1_ref, b0_ref, b1_ref, bias_ref, ya_ref, yb_ref, wo_ref, wp_ref,
             dgl_ref, dya_ref, dyb_ref, db_ref, da_ref, du_ref):
        dm = lax.dot_general(dx_ref[...], w_ref[...], nt_dims, preferred_element_type=F32)
        parts = []
        for br, (lo_ref, hi_ref, y_ref, dy_ref, wb_ref, dsrc_ref) in enumerate((
                (a0_ref, a1_ref, ya_ref, dya_ref, wo_ref, da_ref), (b0_ref, b1_ref, yb_ref, dyb_ref, wp_ref, du_ref))):
            logits = jnp.concatenate([lo_ref[...], hi_ref[...]], axis=1)
            gate = jax.nn.sigmoid(logits + bias_ref[br])
            dy = (dm * gate).astype(BF16)
            dy_ref[...] = dy
            dsrc = lax.dot_general(dy[:, 0:blk], wb_ref[0], nt_dims, preferred_element_type=F32)
            for j in range(1, n_blk):
                dsrc = dsrc + lax.dot_general(dy[:, j * blk:(j + 1) * blk], wb_ref[j], nt_dims,
                                              preferred_element_type=F32)
            dsrc_ref[...] = dsrc
            dgl = dm * y_ref[...] * gate * (1.0 - gate)
            dgl_ref[:, br * d:(br + 1) * d] = dgl.astype(BF16)
            parts.append(jnp.sum(dgl, axis=0, keepdims=True))
        part = jnp.concatenate(parts, axis=0)
        first = pl.program_id(0) == 0

        @pl.when(first)
        def _():
            db_ref[...] = part

        @pl.when(jnp.logical_not(first))
        def _():
            db_ref[...] += part

    row = pl.BlockSpec((tm, d), lambda i: (i, 0))
    logit_blk = lambda k: pl.BlockSpec((tm, half), functools.partial(lambda i, k: (i, c0 + k), k=k))
    blocks = pl.BlockSpec(w_o.shape, lambda i: (0, 0, 0))
    src_row = pl.BlockSpec((tm, k), lambda i: (i, 0))
    return _pallas(
        body, name=name, grid=(s // tm,),
        in_specs=[row, pl.BlockSpec((d, d), lambda i: (0, 0)), logit_blk(0), logit_blk(1), logit_blk(2),
                  logit_blk(3), pl.BlockSpec((2, 1, d), lambda i: (0, 0, 0)), row, row, blocks, blocks],
        out_specs=[pl.BlockSpec((tm, 2 * d), lambda i: (i, 0)), row, row, pl.BlockSpec((2, d), lambda i: (0, 0)),
                   src_row, src_row],
        out_shape=[jax.ShapeDtypeStruct((s, 2 * d), BF16), jax.ShapeDtypeStruct((s, d), BF16),
                   jax.ShapeDtypeStruct((s, d), BF16), jax.ShapeDtypeStruct((2, d), F32),
                   jax.ShapeDtypeStruct((s, k), F32), jax.ShapeDtypeStruct((s, k), F32)],
        operands=[dx1, w_out, proj, proj, proj, proj, bg, y_a, y_b, w_o, w_pw], after=after)


def _ffn_in_swiglu(h2, w_blocked, name):
    s, k = h2.shape
    nblk, _, tn = w_blocked.shape
    ff = nblk // 2 * tn
    tm = 512

    def body(a_ref, wg_ref, wu_ref, g_ref, u_ref, act_ref):
        a = a_ref[...]
        gt = jnp.dot(a, wg_ref[...], preferred_element_type=F32)
        up = jnp.dot(a, wu_ref[...], preferred_element_type=F32)
        g_ref[...] = gt
        u_ref[...] = up
        act_ref[...] = (gt * jax.nn.sigmoid(gt) * up).astype(BF16)

    out = pl.BlockSpec((tm, tn), lambda j, i: (i, j))
    return pl.pallas_call(
        body, name=name, grid=(nblk // 2, s // tm),
        in_specs=[pl.BlockSpec((tm, k), lambda j, i: (i, 0)),
                  pl.BlockSpec((None, k, tn), lambda j, i: (j, 0, 0)),
                  pl.BlockSpec((None, k, tn), lambda j, i: (nblk // 2 + j, 0, 0))],
        out_specs=[out, out, out],
        out_shape=[jax.ShapeDtypeStruct((s, ff), F32), jax.ShapeDtypeStruct((s, ff), F32),
                   jax.ShapeDtypeStruct((s, ff), BF16)],
        compiler_params=_params(),
    )(h2, w_blocked, w_blocked)


def _ffn_out_bwd_swiglu(dy, w_ffn_out, gate, up, name, rider=None):
    s, d = dy.shape
    ff = gate.shape[1]
    tm = 256
    nt_dims = (((1,), (1,)), ((), ()))

    def body(dy_ref, w_ref, g_ref, u_ref, o_ref):
        dv = lax.dot_general(dy_ref[...], w_ref[...], nt_dims, preferred_element_type=F32)
        gt = g_ref[...]
        sg = jax.nn.sigmoid(gt)
        o_ref[:, 0:ff] = (dv * u_ref[...] * (sg * (1.0 + gt * (1.0 - sg)))).astype(BF16)
        o_ref[:, ff:2 * ff] = (dv * gt * sg).astype(BF16)

    row = pl.BlockSpec((tm, ff), lambda i: (i, 0))
    return _pallas(
        body, name=name, grid=(s // tm,),
        in_specs=[pl.BlockSpec((tm, d), lambda i: (i, 0)), pl.BlockSpec((ff, d), lambda i: (0, 0)), row, row],
        out_specs=pl.BlockSpec((tm, 2 * ff), lambda i: (i, 0)),
        out_shape=jax.ShapeDtypeStruct((s, 2 * ff), BF16), operands=[dy, w_ffn_out, gate, up], rider=rider)


def _ffn_out_loss(act, w_ffn_out, x1, target, name):
    s, k = act.shape
    d = w_ffn_out.shape[1]
    tm = 512

    def body(a_ref, w_ref, x1_ref, t_ref, dy_ref, dyb_ref, loss_ref, acc):
        y = x1_ref[...] + jnp.dot(a_ref[...], w_ref[...], preferred_element_type=F32)
        diff = y - t_ref[...]
        dy = diff * (1.0 / d)
        dy_ref[...] = dy
        dyb_ref[...] = dy.astype(BF16)
        part = jnp.sum((diff * diff).reshape(tm // 8, 8, d), axis=0)
        i = pl.program_id(0)

        @pl.when(i == 0)
        def _():
            acc[...] = part

        @pl.when(i > 0)
        def _():
            acc[...] += part

        @pl.when(i == pl.num_programs(0) - 1)
        def _():
            loss_ref[...] = (0.5 / d) * jnp.sum(jnp.sum(acc[...], axis=1, keepdims=True), axis=0, keepdims=True)

    row = pl.BlockSpec((tm, d), lambda i: (i, 0))
    return pl.pallas_call(
        body, name=name, grid=(s // tm,),
        in_specs=[pl.BlockSpec((tm, k), lambda i: (i, 0)), pl.BlockSpec((k, d), lambda i: (0, 0)), row, row],
        out_specs=[row, row, pl.BlockSpec((1, 1), lambda i: (0, 0))],
        out_shape=[jax.ShapeDtypeStruct((s, d), F32), jax.ShapeDtypeStruct((s, d), BF16),
                   jax.ShapeDtypeStruct((1, 1), F32)],
        scratch_shapes=[pltpu.VMEM((8, d), F32)], compiler_params=_params(),
    )(act, w_ffn_out, x1, target)


LATE_GATHER = ("w_o_attn", "w_pw_conv", "w_out", "w_ffn_in", "w_ffn_out")
EARLY_REDUCE = LATE_GATHER


def _blocks_by_half(g):
    if g.ndim == 2:
        g = g.reshape(N_CHIPS, g.shape[0] // N_CHIPS, g.shape[1])
    return g.reshape(N_CHIPS, 2, g.shape[1] // 2, g.shape[2])


def _forward_backward(x, pos_col, target, wts, first_gather, late_bufs, pos_arr):
    wts = dict(wts)
    consts = _rope_consts()
    bd = consts[3]
    qw2 = jnp.tile(wts["q_norm_w"], (1, LANES // HEAD_DIM))
    kw2 = jnp.tile(wts["k_norm_w"], (1, LANES // HEAD_DIM))
    qkv_w = 3 * N_SLOT_HEADS * HEAD_DIM
    conv_col0 = 3 * qkv_w

    h = _rmsnorm_fwd(x, wts["norm1_w"], "rms1_fwd")
    slot_order = jnp.bitwise_xor(pos_arr[1], jnp.asarray([0, 2, 1, 3], jnp.int32))
    blocked = lambda buf: buf.reshape(N_CHIPS, -1, buf.shape[3])
    near = first_gather
    proj = _proj_by_slot(h, blocked(near[2][0]), slot_order, 0, 1, "mm_proj_own", after=list(late_bufs))
    near = _split_middle(near, after=[proj], name="allgather_w_in_near_forward")
    far, _ = _split_start(_gather_both_legs_rider(near[2][:1], near[2][1:], FAR_CHIPS), "allgather_w_in_far_start")
    _, bufs = _split_wait((near[0], near[1], far[2], near[3]), after=[], name="allgather_w_in_near_wait")
    proj = _proj_by_slot(h, blocked(bufs[0]), slot_order, 1, len(NEAR_CHIPS), "mm_proj_near", proj=proj)
    far = _split_middle((far[0], far[1], bufs, far[3]), after=[proj], name="allgather_w_in_far_forward")
    (w_in_buf, conv_w_buf, b_gate_buf), _ = _split_wait(far, after=[], name="allgather_w_in_far_wait")
    wts["w_in"] = w_in_buf.reshape(N_CHIPS, -1, w_in_buf.shape[3])
    wts["conv_w"] = conv_w_buf.transpose(1, 0, 2).reshape(CONV_WIDTH, -1)
    wts["b_gate"] = b_gate_buf.transpose(1, 0, 2).reshape(2, 1, -1)
    ch = wts["conv_w"].shape[1]
    gate_col0 = conv_col0 + 2 * ch
    n_mix = LATE_GATHER.index("w_ffn_in")
    mix_rider, ffn_rider = _gather_ici_rider(late_bufs[:n_mix], []), _gather_ici_rider(late_bufs[n_mix:], [])
    both, started = _split_start(_riders_together(mix_rider, ffn_rider), "late_gather_start", after=[wts["w_in"]])
    n_sem, n_buf = len(mix_rider.scratch), len(mix_rider.operands)
    mix_gather = (mix_rider, both[1][:n_sem], both[2][:n_buf], [])
    ffn_gather = (ffn_rider, both[1][n_sem:], both[2][n_buf:], [])
    proj = _proj_by_slot(h, wts["w_in"], slot_order, 1 + len(NEAR_CHIPS), len(FAR_CHIPS), "mm_proj_far", proj=proj,
                         after=[started])
    qn, kn = _qk_fwd(proj, pos_col, qw2, kw2, consts, "qk_fwd")
    attn, lse, attn_b = _attn_fwd(qn, kn, proj, "attn_fwd")

    def gathered(names, bufs):
        for n, buf in zip(names, bufs):
            full = buf.reshape(N_CHIPS, -1, buf.shape[3])
            wts[n] = full.reshape(-1, full.shape[2]) if n in ROW_SHARDED else full

    mix_bufs, _ = _split_wait(mix_gather, after=[attn_b], name="late_gather_mix_wait")
    (u0, uc), mix_bufs = _conv_fwd(proj, wts["conv_w"], wts["conv_b"], conv_col0, "conv_fwd",
                                   rider=_gather_forward_rider(mix_bufs))
    gathered(LATE_GATHER[:n_mix], mix_bufs)
    u3 = _ln_silu_fwd(uc, wts["conv_ln_w"], wts["conv_ln_b"], "ln_fwd")
    ffn_bufs, _ = _split_wait(ffn_gather, after=[u3], name="late_gather_ffn_wait")
    (y_a, y_b, mixed, x1, h2), ffn_bufs = _mix_out_proj(
        attn_b, u3, wts["w_o_attn"], wts["w_pw_conv"], proj, wts["b_gate"], gate_col0, wts["w_out"], x,
        wts["norm2_w"], "mix_x1_rms2", rider=_gather_forward_rider(ffn_bufs))
    gathered(LATE_GATHER[n_mix:], ffn_bufs)
    gate, up, act = _ffn_in_swiglu(h2, wts["w_ffn_in"], "mm_gu_swiglu")
    dy, dy_b16, loss = _ffn_out_loss(act, wts["w_ffn_out"], x1, target, "mm_x2_loss")

    g = {}
    by_chip = {}

    def pair_add(n, blocks, received):
        return _add_own_half(blocks, received, pos_arr, f"grads_pair_add_{n}")

    g_ffn_out = _blocks_by_half(
        _matmul(act, dy_b16, mode="tn", tm=1408, tn=1024, tk=2048, out_dtype=F32, name="mm_dwffnout"))
    dgu, (received,) = _ffn_out_bwd_swiglu(dy_b16, wts["w_ffn_out"], gate, up, "mm_dact_swiglu_bwd",
                                           rider=_pair_exchange_rider([g_ffn_out], halved=True))
    to_send, own = pair_add("w_ffn_out", g_ffn_out, received)
    (dx1, dx1_b16, g["norm2_w"]), (by_chip["w_ffn_out"],) = _matmul_nt_rmsnorm_bwd(
        dgu, wts["w_ffn_in"], x1, wts["norm2_w"], dy, "mm_dh2_rms2_bwd", rider=_chip_exchange_rider([to_send], [own]))
    g_ffn_in = _blocks_by_half(_matmul(h2, dgu, mode="tn", tm=512, tn=1408, tk=2048, out_dtype=F32,
                                       name="mm_dwffnin", out_blocked=N_CHIPS, cols_outer=True))
    exchanging, started = _split_start(_pair_exchange_rider([g_ffn_in], halved=True), "grads_ffn_in_pair_start")
    g["w_out"] = _matmul(mixed, dx1_b16, mode="tn", tm=512, tn=1024, tk=2048, out_dtype=F32, name="mm_dwout",
                         after=[started])
    dgl, dy_a, dy_b, g["b_gate"], dattn, du3 = _out_proj_bwd_gates(
        dx1_b16, wts["w_out"], proj, wts["b_gate"], y_a, y_b, wts["w_o_attn"], wts["w_pw_conv"], gate_col0,
        "mix_bwd")
    (received,), (g_ffn_in,) = _split_wait(exchanging, after=[dgl], name="grads_ffn_in_pair_wait")
    ffn_in_to_send, ffn_in_own = pair_add("w_ffn_in", g_ffn_in, received)
    g["w_o_attn"] = _matmul(attn_b, dy_a, mode="tn", tm=512, tn=256, tk=2048, out_dtype=F32, name="mm_dwo",
                            out_blocked=N_CHIPS)
    g["w_pw_conv"] = _matmul(u3, dy_b, mode="tn", tm=512, tn=256, tk=2048, out_dtype=F32, name="mm_dwpw",
                             out_blocked=N_CHIPS)
    duc, g["conv_ln_w"], g["conv_ln_b"] = _ln_silu_bwd(du3, uc, wts["conv_ln_w"], wts["conv_ln_b"], "ln_bwd")

    small3 = ("w_out", "w_o_attn", "w_pw_conv")
    g_small3 = [_blocks_by_half(g.pop(n)) for n in small3]
    (da, db, g["conv_w"], g["conv_b"]), received = _conv_bwd(
        duc, u0, proj, wts["conv_w"], conv_col0, "conv_bwd", rider=_pair_exchange_rider(g_small3, halved=True))
    sums3 = [pair_add(n, gb, rv) for n, gb, rv in zip(small3, g_small3, received)]
    (dqn, dkn, dv), (by_chip["w_ffn_in"],) = _attn_bwd(
        qn, kn, proj, dattn, attn, lse, bd, "attn_bwd",
        rider=_chip_exchange_rider([ffn_in_to_send], [ffn_in_own]))
    (dproj, dqw, dkw), exchanged3 = _qk_bwd(
        dqn, dkn, dv, da, db, dgl, proj, pos_col, qw2, kw2, consts, "qk_bwd",
        rider=_chip_exchange_rider([s[0] for s in sums3], [s[1] for s in sums3]))
    by_chip.update(zip(small3, exchanged3))
    halves = [_sum_chips(by_chip[n], pos_arr, f"grads_chip_sum_{n}") for n in EARLY_REDUCE]
    g["q_norm_w"] = dqw[:, :HEAD_DIM]
    g["k_norm_w"] = dkw[:, :HEAD_DIM]

    c = pos_arr[0]
    rh = h.shape[1] // 2
    h_sibling = lax.dynamic_slice_in_dim(h, (1 - c) * rh, rh, axis=1)
    h_own = lax.dynamic_slice_in_dim(h, c * rh, rh, axis=1)
    g_sibling, shards = _matmul(h_sibling, dproj, mode="tn", tm=rh, tn=1920, tk=2048, out_dtype=F32,
                                name="mm_dwin_sibling", out_blocked=N_CHIPS, rider=_pair_gather_rider(halves))
    reduced = dict(zip(EARLY_REDUCE, shards))
    exchanging, started = _split_start(_pair_exchange_rider([g_sibling], halved=False), "grads_w_in_pair_start")
    g_own = _matmul(h_own, dproj, mode="tn", tm=rh, tn=1920, tk=2048, out_dtype=F32, name="mm_dwin_own",
                    out_blocked=N_CHIPS, after=[started])
    (from_sibling,), _ = _split_wait(exchanging, after=[g_own], name="grads_w_in_pair_wait")
    to_send, own = _add_own_half(g_own, from_sibling, pos_arr, "grads_pair_add_w_in")
    in_flight, started = _split_start(_chip_exchange_rider([to_send], [own]), "grads_w_in_exchange_start")
    grad_x, _, g["norm1_w"] = _matmul_nt_rmsnorm_bwd(dproj, wts["w_in"], x, wts["norm1_w"], dx1, "mm_dh_rms1_bwd",
                                                     after=[started])
    return loss, grad_x, g, reduced, (in_flight, started)


def _mesh_pos():
    return lax.axis_index("x"), lax.axis_index("y"), lax.axis_index("c")


def _other_chips(x, y):
    return [(1 - x, y), (x, 1 - y), (1 - x, 1 - y)]


NEAR_CHIPS, FAR_CHIPS = (0, 1), (2,)


def _cast_into_slot(shard, chip_arr, dtype, name, n_slots=N_CHIPS, after=()):
    r, c = shard.shape
    tr = r // 2 if r % 32 == 0 else r

    def body(chip_ref, s_ref, *refs):
        refs[-1][...] = s_ref[...].astype(dtype)

    return pl.pallas_call(
        body, name=name,
        grid_spec=pltpu.PrefetchScalarGridSpec(
            num_scalar_prefetch=1, grid=(r // tr,),
            in_specs=[pl.BlockSpec((tr, c), lambda i, chip_ref: (i, 0))] + [ANY] * len(after),
            out_specs=pl.BlockSpec((None, tr, c), lambda i, chip_ref: (chip_ref[0], i, 0))),
        out_shape=jax.ShapeDtypeStruct((n_slots, r, c), dtype), compiler_params=_params(),
    )(chip_arr, shard, *after)


GATHER_CHUNKS = 4


def _gather_both_legs_rider(big, small, peers):
    nb = len(big)
    n = nb + len(small)
    nch = GATHER_CHUNKS

    def part(bufs, a, slot, half, ch):
        if a >= nb:
            return bufs[a].at[slot]
        rows = bufs[a].shape[2] // nch
        return bufs[a].at[slot, half, pl.ds(ch * rows, rows)]

    def pieces():
        return [(a, ch, k) for ch in range(nch) for a in range(n) for k in peers if a < nb or ch == 0]

    def ici(bufs, sems, a, ch, k, slot_of_src):
        x, y, c = _mesh_pos()
        px, py = _other_chips(x, y)[k]
        slot = 2 * x + y if slot_of_src == "mine" else 2 * px + py
        return pltpu.make_async_remote_copy(
            src_ref=part(bufs, a, slot, c, ch), dst_ref=part(bufs, a, slot, c, ch), send_sem=sems[0].at[a, ch, k],
            recv_sem=sems[1].at[a, ch, k], device_id=(px, py, c), device_id_type=MESH)

    def forward(bufs, sems, a, ch, k, half):
        x, y, c = _mesh_pos()
        px, py = _other_chips(x, y)[k]
        h = c if half == "mine" else 1 - c
        return pltpu.make_async_remote_copy(
            src_ref=part(bufs, a, 2 * px + py, h, ch), dst_ref=part(bufs, a, 2 * px + py, h, ch),
            send_sem=sems[2].at[a, ch, k], recv_sem=sems[3].at[a, ch, k], device_id=(x, y, 1 - c),
            device_id_type=MESH)

    def start(r_in, bufs, sems):
        for a, ch, k in pieces():
            ici(bufs, sems, a, ch, k, "mine").start()

    def middle(r_in, bufs, sems):
        for a, ch, k in pieces():
            ici(bufs, sems, a, ch, k, "theirs").wait_recv()
            if a < nb:
                forward(bufs, sems, a, ch, k, "mine").start()
        for a, ch, k in pieces():
            ici(bufs, sems, a, ch, k, "mine").wait_send()

    def wait(r_in, bufs, sems):
        for a, ch, k in pieces():
            if a < nb:
                forward(bufs, sems, a, ch, k, "theirs").wait_recv()
        for a, ch, k in pieces():
            if a < nb:
                forward(bufs, sems, a, ch, k, "mine").wait_send()

    ops = list(big) + list(small)
    return _Rider(ops, [jax.ShapeDtypeStruct(o.shape, o.dtype) for o in ops], {i: i for i in range(n)},
                  [pltpu.SemaphoreType.DMA((n, nch, 3)), pltpu.SemaphoreType.DMA((n, nch, 3)),
                   pltpu.SemaphoreType.DMA((nb, nch, 3)), pltpu.SemaphoreType.DMA((nb, nch, 3))],
                  start, wait, middle)


def _comm_call(rider, name):
    def body():
        pass

    return _pallas(body, name=name, grid=(1,), in_specs=[], out_specs=[], out_shape=[], operands=[],
                   rider=rider)[1]


def _gather_ici_rider(big, small):
    nb = len(big)
    n = nb + len(small)

    def copies(bufs, sems):
        x, y, c = _mesh_pos()
        me = 2 * x + y
        part = lambda a, slot: bufs[a].at[slot, c] if a < nb else bufs[a].at[slot]
        out = []
        for a in range(n):
            for k, (px, py) in enumerate(_other_chips(x, y)):
                send = functools.partial(
                    pltpu.make_async_remote_copy,
                    src_ref=part(a, me), dst_ref=part(a, me), send_sem=sems[0].at[a, k],
                    recv_sem=sems[1].at[a, k], device_id=(px, py, c), device_id_type=MESH)
                recv = functools.partial(
                    pltpu.make_async_remote_copy,
                    src_ref=part(a, 2 * px + py), dst_ref=part(a, 2 * px + py), send_sem=sems[0].at[a, k],
                    recv_sem=sems[1].at[a, k], device_id=(px, py, c), device_id_type=MESH)
                out.append((send, recv))
        return out

    def start(r_in, r_out, sems):
        for send, _ in copies(r_out, sems):
            send().start()

    def wait(r_in, r_out, sems):
        cps = copies(r_out, sems)
        for _, recv in cps:
            recv().wait_recv()
        for send, _ in cps:
            send().wait_send()

    ops = list(big) + list(small)
    return _Rider(ops, [jax.ShapeDtypeStruct(o.shape, o.dtype) for o in ops], {i: i for i in range(n)},
                  [pltpu.SemaphoreType.DMA((n, 3)), pltpu.SemaphoreType.DMA((n, 3))], start, wait)


def _gather_forward_rider(big):
    n = len(big)

    def copies(bufs, sems):
        x, y, c = _mesh_pos()
        out = []
        for a in range(n):
            for k, (px, py) in enumerate(_other_chips(x, y)):
                slot = 2 * px + py
                send = functools.partial(
                    pltpu.make_async_remote_copy,
                    src_ref=bufs[a].at[slot, c], dst_ref=bufs[a].at[slot, c], send_sem=sems[0].at[a, k],
                    recv_sem=sems[1].at[a, k], device_id=(x, y, 1 - c), device_id_type=MESH)
                recv = functools.partial(
                    pltpu.make_async_remote_copy,
                    src_ref=bufs[a].at[slot, 1 - c], dst_ref=bufs[a].at[slot, 1 - c], send_sem=sems[0].at[a, k],
                    recv_sem=sems[1].at[a, k], device_id=(x, y, 1 - c), device_id_type=MESH)
                out.append((send, recv))
        return out

    def start(r_in, r_out, sems):
        for send, _ in copies(r_out, sems):
            send().start()

    def wait(r_in, r_out, sems):
        cps = copies(r_out, sems)
        for _, recv in cps:
            recv().wait_recv()
        for send, _ in cps:
            send().wait_send()

    return _Rider(big, [jax.ShapeDtypeStruct(o.shape, o.dtype) for o in big], {i: i for i in range(n)},
                  [pltpu.SemaphoreType.DMA((n, 3)), pltpu.SemaphoreType.DMA((n, 3))], start, wait)


def _pair_exchange_rider(gs, halved):
    n = len(gs)

    def copies(r_in, r_out, sems):
        x, y, c = _mesh_pos()
        return [pltpu.make_async_remote_copy(
            src_ref=r_in[a].at[:, 1 - c] if halved else r_in[a], dst_ref=r_out[a], send_sem=sems[0].at[a],
            recv_sem=sems[1].at[a], device_id=(x, y, 1 - c), device_id_type=MESH) for a in range(n)]

    def start(r_in, r_out, sems):
        for cp in copies(r_in, r_out, sems):
            cp.start()

    def wait(r_in, r_out, sems):
        for cp in copies(r_in, r_out, sems):
            cp.wait()

    return _Rider(gs, [jax.ShapeDtypeStruct((g.shape[0],) + g.shape[-2:], g.dtype) for g in gs], {},
                  [pltpu.SemaphoreType.DMA((n,)), pltpu.SemaphoreType.DMA((n,))], start, wait)


def _chip_exchange_rider(to_send, by_chip, row_range=None):
    n = len(to_send)

    def copies(r_in, r_out, sems):
        x, y, c = _mesh_pos()
        me = 2 * x + y
        rows = (lambda ref: ref) if row_range is None else (lambda ref: ref.at[pl.ds(*row_range)])
        out = []
        for a in range(n):
            for k, (px, py) in enumerate(_other_chips(x, y)):
                send = functools.partial(
                    pltpu.make_async_remote_copy,
                    src_ref=rows(r_in[a].at[2 * px + py]), dst_ref=rows(r_out[a].at[me]),
                    send_sem=sems[0].at[a, k], recv_sem=sems[1].at[a, k], device_id=(px, py, c),
                    device_id_type=MESH)
                recv = functools.partial(
                    pltpu.make_async_remote_copy,
                    src_ref=rows(r_in[a].at[me]), dst_ref=rows(r_out[a].at[2 * px + py]),
                    send_sem=sems[0].at[a, k], recv_sem=sems[1].at[a, k], device_id=(px, py, c),
                    device_id_type=MESH)
                out.append((send, recv))
        return out

    def start(r_in, r_out, sems):
        for send, _ in copies(r_in, r_out, sems):
            send().start()

    def wait(r_in, r_out, sems):
        cps = copies(r_in, r_out, sems)
        for _, recv in cps:
            recv().wait_recv()
        for send, _ in cps:
            send().wait_send()

    return _Rider(list(to_send) + list(by_chip), [jax.ShapeDtypeStruct(b.shape, b.dtype) for b in by_chip],
                  {n + i: i for i in range(n)},
                  [pltpu.SemaphoreType.DMA((n, 3)), pltpu.SemaphoreType.DMA((n, 3))], start, wait)


HBM = pl.BlockSpec(memory_space=pltpu.HBM)
SEM = pl.BlockSpec(memory_space=pltpu.SEMAPHORE)


_IN_FLIGHT = pltpu.CompilerParams(has_side_effects=pltpu.SideEffectType.DATAFLOW_SIDE_EFFECTING)


class _FlatSems:
    def __init__(self, ref, shape):
        self.ref, self.shape = ref, shape

    @property
    def at(self):
        return self

    def __getitem__(self, idx):
        idx = idx if isinstance(idx, tuple) else (idx,)
        flat = 0
        for i, n in zip(idx, self.shape):
            flat = flat * n + i
        return self.ref.at[flat]


def _flat_sem_types(rider):
    return tuple(pltpu.SemaphoreType.DMA((int(np.prod(s.shape)),)) for s in rider.scratch)


def _as_rider_sems(rider, refs):
    return [_FlatSems(r, s.shape) for r, s in zip(refs, rider.scratch)]


def _split_start(rider, name, after=()):
    n_in, n_out, n_sem = len(rider.operands), len(rider.out_shapes), len(rider.scratch)
    n_after = len(after)
    fresh = [j for j in range(n_out) if j not in rider.aliases.values()]
    by_out = {j: i for i, j in rider.aliases.items()}

    def body(*refs):
        r_in = refs[:n_in]
        refs = refs[n_in + n_after:]
        sems = refs[:n_sem]
        thru = refs[n_sem:n_sem + n_in]
        fresh_refs = refs[n_sem + n_in:n_sem + n_in + len(fresh)]
        token = refs[-1]
        r_out = [thru[by_out[j]] if j in by_out else fresh_refs[fresh.index(j)] for j in range(n_out)]
        rider.start(r_in, r_out, _as_rider_sems(rider, sems))
        token[...] = jnp.zeros_like(token)

    res = pl.pallas_call(
        body, name=name,
        out_shape=_flat_sem_types(rider) + tuple(pltpu.HBM(o.shape, o.dtype) for o in rider.operands)
        + tuple(pltpu.HBM(rider.out_shapes[j].shape, rider.out_shapes[j].dtype) for j in fresh)
        + (jax.ShapeDtypeStruct((8, LANES), F32),),
        in_specs=(HBM,) * n_in + (ANY,) * n_after,
        out_specs=(SEM,) * n_sem + (HBM,) * (n_in + len(fresh)) + (pl.BlockSpec(memory_space=pltpu.VMEM),),
        input_output_aliases={i: n_sem + i for i in range(n_in)}, compiler_params=_IN_FLIGHT,
    )(*[pltpu.with_memory_space_constraint(o, pltpu.HBM) for o in rider.operands], *after)
    return (rider, res[:n_sem], res[n_sem:n_sem + n_in], res[n_sem + n_in:-1]), res[-1]


def _split_continue(handles, after, name, phase):
    rider, sems, thru, fresh_arrays = handles
    n_in, n_out, n_sem = len(rider.operands), len(rider.out_shapes), len(rider.scratch)
    fresh = [j for j in range(n_out) if j not in rider.aliases.values()]
    by_out = {j: i for i, j in rider.aliases.items()}
    n_data = n_in + len(fresh)

    def body(*refs):
        r_in = refs[:n_in]
        fresh_refs = refs[n_in:n_data]
        sem_refs = refs[n_data:n_data + n_sem]
        r_out = [r_in[by_out[j]] if j in by_out else fresh_refs[fresh.index(j)] for j in range(n_out)]
        phase(r_in, r_out, _as_rider_sems(rider, sem_refs))

    data = list(thru) + list(fresh_arrays)
    return pl.pallas_call(
        body, name=name, out_shape=tuple(pltpu.HBM(d.shape, d.dtype) for d in data),
        in_specs=(HBM,) * n_data + (SEM,) * n_sem + (ANY,) * len(after), out_specs=(HBM,) * n_data,
        input_output_aliases={i: i for i in range(n_data)}, compiler_params=_IN_FLIGHT,
    )(*data, *sems, *after)


def _split_middle(handles, after, name):
    rider, sems, thru, _ = handles
    res = _split_continue(handles, after, name, rider.middle)
    return rider, sems, res[:len(thru)], res[len(thru):]


def _split_wait(handles, after, name):
    rider = handles[0]
    n_in, n_out = len(rider.operands), len(rider.out_shapes)
    fresh = [j for j in range(n_out) if j not in rider.aliases.values()]
    by_out = {j: i for i, j in rider.aliases.items()}
    res = _split_continue(handles, after, name, rider.wait)
    return [res[by_out[j]] if j in by_out else res[n_in + fresh.index(j)] for j in range(n_out)], res[:n_in]


def _pair_gather_rider(bufs):
    n = len(bufs)

    def copies(r_out, sems):
        x, y, c = _mesh_pos()
        out = []
        for a in range(n):
            send = functools.partial(
                    pltpu.make_async_remote_copy,
                src_ref=r_out[a].at[c], dst_ref=r_out[a].at[c], send_sem=sems[0].at[a],
                recv_sem=sems[1].at[a], device_id=(x, y, 1 - c), device_id_type=MESH)
            recv = functools.partial(
                    pltpu.make_async_remote_copy,
                src_ref=r_out[a].at[1 - c], dst_ref=r_out[a].at[1 - c], send_sem=sems[0].at[a],
                recv_sem=sems[1].at[a], device_id=(x, y, 1 - c), device_id_type=MESH)
            out.append((send, recv))
        return out

    def start(r_in, r_out, sems):
        for send, _ in copies(r_out, sems):
            send().start()

    def wait(r_in, r_out, sems):
        cps = copies(r_out, sems)
        for _, recv in cps:
            recv().wait_recv()
        for send, _ in cps:
            send().wait_send()

    return _Rider(bufs, [jax.ShapeDtypeStruct(b.shape, b.dtype) for b in bufs], {i: i for i in range(n)},
                  [pltpu.SemaphoreType.DMA((n,)), pltpu.SemaphoreType.DMA((n,))], start, wait)


def _add_own_half(g, recv, pos_arr, name):
    nb, rh, cols = g.shape[0], g.shape[-2], g.shape[-1]

    def body(pos_ref, g_ref, r_ref, send_ref, own_ref):
        s = (g_ref[...] + r_ref[...]).astype(BF16)
        send_ref[...] = s

        @pl.when(pl.program_id(0) == pos_ref[1])
        def _():
            own_ref[...] = s

    blk = pl.BlockSpec((None, rh, cols), lambda j, pos_ref: (j, 0, 0))
    g_spec = blk if g.ndim == 3 else pl.BlockSpec((None, None, rh, cols),
                                                   lambda j, pos_ref: (j, pos_ref[0], 0, 0))
    shape = jax.ShapeDtypeStruct((nb, rh, cols), BF16)
    return pl.pallas_call(
        body, name=name,
        grid_spec=pltpu.PrefetchScalarGridSpec(
            num_scalar_prefetch=1, grid=(nb,), in_specs=[g_spec, blk],
            out_specs=[blk, pl.BlockSpec((None, rh, cols), lambda j, pos_ref: (pos_ref[1], 0, 0))]),
        out_shape=[shape, shape], compiler_params=_params(),
    )(pos_arr, g, recv)


def _sum_chips(gath, pos_arr, name):
    nb, rh, cols = gath.shape

    def body(pos_ref, a_ref, b_ref, c_ref, d_ref, o_ref):
        del pos_ref
        o_ref[...] = ((a_ref[...].astype(F32) + b_ref[...].astype(F32)) + c_ref[...].astype(F32)) \
            + d_ref[...].astype(F32)

    tr = rh // 2 if (rh // 2) % 16 == 0 else rh
    specs = [pl.BlockSpec((None, tr, cols), functools.partial(lambda i, pos_ref, j: (j, i, 0), j=j))
             for j in range(nb)]
    return pl.pallas_call(
        body, name=name,
        grid_spec=pltpu.PrefetchScalarGridSpec(
            num_scalar_prefetch=1, grid=(rh // tr,), in_specs=specs,
            out_specs=pl.BlockSpec((None, tr, cols), lambda i, pos_ref: (pos_ref[0], i, 0))),
        out_shape=jax.ShapeDtypeStruct((2, rh, cols), F32), compiler_params=_params(),
    )(pos_arr, gath, gath, gath, gath)


N_DEVICES = 8


def _small_gather_rider(buf):
    def copies(r_out, sems):
        x, y, c = _mesh_pos()
        me = 4 * x + 2 * y + c
        out = []
        for r in range(1, N_DEVICES):
            px = 1 - x if r & 4 else x
            py = 1 - y if r & 2 else y
            pc = 1 - c if r & 1 else c
            out.append(pltpu.make_async_remote_copy(
                src_ref=r_out[0].at[me], dst_ref=r_out[0].at[me], send_sem=sems[0].at[r - 1],
                recv_sem=sems[1].at[r - 1], device_id=(px, py, pc), device_id_type=MESH))
        return out

    def start(r_in, r_out, sems):
        for cp in copies(r_out, sems):
            cp.start()

    def wait(r_in, r_out, sems):
        cps = copies(r_out, sems)
        for cp in cps:
            cp.wait_recv()
        for cp in cps:
            cp.wait_send()

    return _Rider([buf], [jax.ShapeDtypeStruct(buf.shape, buf.dtype)], {0: 0},
                  [pltpu.SemaphoreType.DMA((N_DEVICES - 1,)), pltpu.SemaphoreType.DMA((N_DEVICES - 1,))],
                  start, wait)


def _sum_devices(buf, name):
    def body(b_ref, o_ref):
        acc = b_ref[0]
        for i in range(1, N_DEVICES):
            acc = acc + b_ref[i]
        o_ref[...] = acc

    return _pallas(body, name=name, grid=(1,), in_specs=[pl.BlockSpec(buf.shape, lambda i: (0, 0, 0))],
                   out_specs=pl.BlockSpec(buf.shape[1:], lambda i: (0, 0)),
                   out_shape=jax.ShapeDtypeStruct(buf.shape[1:], F32), operands=[buf])


def _adamw_math(w, g, m, v):
    m = ADAM_B1 * m + (1.0 - ADAM_B1) * g
    v = ADAM_B2 * v + (1.0 - ADAM_B2) * (g * g)
    m_hat = m / (1.0 - ADAM_B1 ** ADAM_STEP)
    v_hat = v / (1.0 - ADAM_B2 ** ADAM_STEP)
    delta = -ADAM_LR * (m_hat / (jnp.sqrt(v_hat) + ADAM_EPS) + ADAM_WD * w)
    return delta, m, v


def _adamw(w, g, m, v, name, after=()):
    r, c = w.shape
    tr = next(t for t in (256, 352, 128, 64) if r % t == 0 and r >= 2 * t)

    def body(w_ref, g_ref, m_ref, v_ref, go_ref, d_ref, mo_ref, vo_ref):
        gv = g_ref[...]
        d, mn, vn = _adamw_math(w_ref[...], gv, m_ref[...], v_ref[...])
        go_ref[...] = gv
        d_ref[...] = d
        mo_ref[...] = mn
        vo_ref[...] = vn

    blk = pl.BlockSpec((tr, c), lambda i: (i, 0))
    return _pallas(body, name=name, grid=(r // tr,), in_specs=[blk] * 4, out_specs=[blk] * 4,
                   out_shape=[jax.ShapeDtypeStruct((r, c), F32)] * 4, operands=[w, g, m, v], after=after)


def _adamw_small(ws, gs, ms, vs, name):
    n = len(ws)

    def body(*refs):
        w_r, g_r, m_r, v_r = refs[:n], refs[n:2 * n], refs[2 * n:3 * n], refs[3 * n:4 * n]
        d_o, m_o, v_o = refs[4 * n:5 * n], refs[5 * n:6 * n], refs[6 * n:7 * n]
        for i in range(n):
            d, mn, vn = _adamw_math(w_r[i][...], g_r[i][...], m_r[i][...], v_r[i][...])
            d_o[i][...] = d
            m_o[i][...] = mn
            v_o[i][...] = vn

    specs = [pl.BlockSpec(w.shape, lambda i: (0, 0)) for w in ws]
    shapes = [jax.ShapeDtypeStruct(w.shape, F32) for w in ws]
    outs = pl.pallas_call(
        body, name=name, grid=(1,), in_specs=specs * 4, out_specs=specs * 3, out_shape=shapes * 3,
        compiler_params=_params(),
    )(*ws, *gs, *ms, *vs)
    return outs[:n], outs[n:2 * n], outs[2 * n:]


BIG = ("w_in", "w_o_attn", "w_pw_conv", "w_out", "w_ffn_in", "w_ffn_out")
ROW_SHARDED = ("w_out", "w_ffn_out")
SMALL = ("norm1_w", "b_gate", "q_norm_w", "k_norm_w", "conv_w", "conv_b", "conv_ln_w", "conv_ln_b", "norm2_w")
ORDER = ("norm1_w", "w_in", "b_gate", "q_norm_w", "k_norm_w", "w_o_attn", "conv_w", "conv_b", "conv_ln_w",
         "conv_ln_b", "w_pw_conv", "w_out", "norm2_w", "w_ffn_in", "w_ffn_out")
PACK_TILE = 8 * LANES


def _pack_small(parts):
    rows = []
    for p in parts:
        flat = p.reshape(-1)
        pad = (-flat.shape[0]) % PACK_TILE
        rows.append(jnp.pad(flat, (0, pad)).reshape(-1, LANES))
    return jnp.concatenate(rows, axis=0)


def _unpack_small(packed, shapes):
    out, row = [], 0
    for shp in shapes:
        size = int(np.prod(shp))
        nrow = -(-size // PACK_TILE) * (PACK_TILE // LANES)
        out.append(packed[row:row + nrow].reshape(-1)[:size].reshape(shp))
        row += nrow
    return out


def kernel(x, positions, norm1_w, w_in, b_gate, q_norm_w, k_norm_w, w_o_attn, conv_w, conv_b, conv_ln_w, conv_ln_b, w_pw_conv, w_out, norm2_w, w_ffn_in, w_ffn_out, loss_target, m_norm1_w, m_w_in, m_b_gate, m_q_norm_w, m_k_norm_w, m_w_o_attn, m_conv_w, m_conv_b, m_conv_ln_w, m_conv_ln_b, m_w_pw_conv, m_w_out, m_norm2_w, m_w_ffn_in, m_w_ffn_out, v_norm1_w, v_w_in, v_b_gate, v_q_norm_w, v_k_norm_w, v_w_o_attn, v_conv_w, v_conv_b, v_conv_ln_w, v_conv_ln_b, v_w_pw_conv, v_w_out, v_norm2_w, v_w_ffn_in, v_w_ffn_out):
    w = dict(norm1_w=norm1_w, w_in=w_in, b_gate=b_gate, q_norm_w=q_norm_w, k_norm_w=k_norm_w, w_o_attn=w_o_attn,
             conv_w=conv_w, conv_b=conv_b, conv_ln_w=conv_ln_w, conv_ln_b=conv_ln_b, w_pw_conv=w_pw_conv,
             w_out=w_out, norm2_w=norm2_w, w_ffn_in=w_ffn_in, w_ffn_out=w_ffn_out)
    m = dict(norm1_w=m_norm1_w, w_in=m_w_in, b_gate=m_b_gate, q_norm_w=m_q_norm_w, k_norm_w=m_k_norm_w,
             w_o_attn=m_w_o_attn, conv_w=m_conv_w, conv_b=m_conv_b, conv_ln_w=m_conv_ln_w,
             conv_ln_b=m_conv_ln_b, w_pw_conv=m_w_pw_conv, w_out=m_w_out, norm2_w=m_norm2_w,
             w_ffn_in=m_w_ffn_in, w_ffn_out=m_w_ffn_out)
    v = dict(norm1_w=v_norm1_w, w_in=v_w_in, b_gate=v_b_gate, q_norm_w=v_q_norm_w, k_norm_w=v_k_norm_w,
             w_o_attn=v_w_o_attn, conv_w=v_conv_w, conv_b=v_conv_b, conv_ln_w=v_conv_ln_w,
             conv_ln_b=v_conv_ln_b, w_pw_conv=v_w_pw_conv, w_out=v_w_out, norm2_w=v_norm2_w,
             w_ffn_in=v_w_ffn_in, w_ffn_out=v_w_ffn_out)
    cx, cy, cc = _mesh_pos()
    chip = 2 * cx + cy

    chip_arr = chip.reshape(1).astype(jnp.int32)
    pos_arr = jnp.stack([cc, chip]).astype(jnp.int32)
    halves = lambda buf: buf.reshape(N_CHIPS, 2, buf.shape[1] // 2, buf.shape[2])
    w_in_buf = halves(_cast_into_slot(w["w_in"][0], chip_arr, BF16, "cast_w_in"))
    small_bufs = [_cast_into_slot(w[n][0], chip_arr, F32, f"slot_{n}") for n in ("conv_w", "b_gate")]
    first_gather, started = _split_start(_gather_both_legs_rider([w_in_buf], small_bufs, NEAR_CHIPS),
                                         "allgather_w_in_near_start")
    late_bufs = [halves(_cast_into_slot(w[n][0], chip_arr, BF16, f"cast_{n}", after=[started]))
                 for n in LATE_GATHER]
    wts = dict(norm1_w=norm1_w, q_norm_w=q_norm_w, k_norm_w=k_norm_w, conv_b=conv_b, conv_ln_w=conv_ln_w,
               conv_ln_b=conv_ln_b, norm2_w=norm2_w)

    loss, grad_x, g, reduced, w_in_in_flight = _forward_backward(
        x[0], positions.reshape(-1, 1), loss_target[0], wts, first_gather, late_bufs, pos_arr)
    grads = {n: b.reshape(-1, b.shape[2]) for n, b in reduced.items()}

    w_in_in_flight, started = w_in_in_flight
    delta, new_m, new_v = {}, {}, {}
    for n in EARLY_REDUCE:
        grads[n], delta[n], new_m[n], new_v[n] = _adamw(w[n][0], grads[n], m[n][0], v[n][0], f"adamw_{n}",
                                                        after=[started])
    small_parts = [loss] + [g[n] for n in SMALL]
    small_shapes = [p.shape for p in small_parts]
    device_arr = (4 * cx + 2 * cy + cc).reshape(1).astype(jnp.int32)
    small_buf = _cast_into_slot(_pack_small(small_parts), device_arr, F32, "slot_small", n_slots=N_DEVICES)

    (by_chip_w_in,), _ = _split_wait(w_in_in_flight, after=[delta[n] for n in EARLY_REDUCE] + [small_buf],
                                     name="grads_w_in_exchange_wait")
    half_w_in = _sum_chips(by_chip_w_in, pos_arr, "grads_chip_sum_w_in")
    shard_w_in, small_buf = _comm_call(
        _riders_together(_pair_gather_rider([half_w_in]), _small_gather_rider(small_buf)),
        "grads_pair_gather_w_in_small_gather")
    summed = _sum_devices(small_buf, "small_sum")
    reduced = _unpack_small(summed, small_shapes)
    loss_total = reduced[0].reshape(())
    for n, r in zip(SMALL, reduced[1:]):
        grads[n] = r
    ch_shard = conv_w.shape[2]
    grads["conv_w"] = lax.dynamic_slice_in_dim(grads["conv_w"], chip * ch_shard, ch_shard, axis=1)
    d_shard = b_gate.shape[2]
    grads["b_gate"] = lax.dynamic_slice_in_dim(grads["b_gate"], chip * d_shard, d_shard, axis=1)

    grads["w_in"], delta["w_in"], new_m["w_in"], new_v["w_in"] = _adamw(
        w["w_in"][0], shard_w_in.reshape(-1, shard_w_in.shape[2]), m["w_in"][0], v["w_in"][0], "adamw_w_in")
    flat2 = lambda a: a.reshape(-1, a.shape[-1])
    d_s, m_s, v_s = _adamw_small([flat2(w[n]) for n in SMALL], [flat2(grads[n]) for n in SMALL],
                                 [flat2(m[n]) for n in SMALL], [flat2(v[n]) for n in SMALL], "adamw_small")
    for i, n in enumerate(SMALL):
        delta[n], new_m[n], new_v[n] = d_s[i], m_s[i], v_s[i]

    shaped = lambda d, n: d[n].reshape(w[n].shape)
    return (loss_total, grad_x[None], *[shaped(grads, n) for n in ORDER], *[shaped(delta, n) for n in ORDER],
            *[shaped(new_m, n) for n in ORDER], *[shaped(new_v, n) for n in ORDER])
```

```python
import functools

import numpy as np
import jax
import jax.numpy as jnp
from jax import lax
from jax.experimental import pallas as pl
from jax.experimental.pallas import tpu as pltpu

F32 = jnp.float32
BF16 = jnp.bfloat16
MESH = pl.DeviceIdType.MESH
ANY = pl.BlockSpec(memory_space=pl.ANY)

HEAD_DIM = 64
N_SLOT_HEADS = 8
DILATIONS = (1, 4, 16)
HALF_SPAN = 64
ROPE_THETA = 500000.0
ROT_DIM = 16
CONV_WIDTH = 31
EPS = 1e-6
NEG_INF = -1e30
ADAM_LR, ADAM_B1, ADAM_B2, ADAM_EPS, ADAM_WD, ADAM_STEP = 0.001, 0.9, 0.999, 1e-08, 0.01, 10

LANES = 128
QBLK = 128
KWIN = QBLK + 2 * HALF_SPAN
VMEM_LIMIT = 48 * 1024 * 1024
N_CHIPS = 4


def _params(**kw):
    return pltpu.CompilerParams(vmem_limit_bytes=VMEM_LIMIT, **kw)


class _Rider:
    def __init__(self, operands, out_shapes, aliases, scratch, start, wait, middle=None):
        self.operands, self.out_shapes, self.aliases = list(operands), list(out_shapes), dict(aliases)
        self.scratch, self.start, self.wait = list(scratch), start, wait
        self.middle = middle


def _riders_together(a, b):
    n_in, n_out, n_sc = len(a.operands), len(a.out_shapes), len(a.scratch)
    aliases = dict(a.aliases)
    aliases.update({n_in + src: n_out + dst for src, dst in b.aliases.items()})

    def start(r_in, r_out, r_sc):
        a.start(r_in[:n_in], r_out[:n_out], r_sc[:n_sc])
        b.start(r_in[n_in:], r_out[n_out:], r_sc[n_sc:])

    def wait(r_in, r_out, r_sc):
        a.wait(r_in[:n_in], r_out[:n_out], r_sc[:n_sc])
        b.wait(r_in[n_in:], r_out[n_out:], r_sc[n_sc:])

    return _Rider(a.operands + b.operands, a.out_shapes + b.out_shapes, aliases, a.scratch + b.scratch, start, wait)


def _pallas(body, *, name, grid, in_specs, out_specs, out_shape, operands, scratch_shapes=(), aliases=None,
            rider=None, after=()):
    single = not isinstance(out_specs, (list, tuple))
    out_specs_l = [out_specs] if single else list(out_specs)
    out_shape_l = [out_shape] if single else list(out_shape)
    aliases = dict(aliases or {})

    def call(fn, all_in_specs, all_out_specs, all_out_shape, all_scratch, all_aliases, all_operands):
        return pl.pallas_call(
            fn, name=name, grid=grid, in_specs=all_in_specs, out_specs=all_out_specs, out_shape=all_out_shape,
            scratch_shapes=all_scratch, input_output_aliases=all_aliases, compiler_params=_params(),
        )(*all_operands)

    if rider is None:
        n_main = len(in_specs)

        def ordered(*refs):
            body(*refs[:n_main], *refs[n_main + len(after):])

        res = call(ordered if after else body, list(in_specs) + [ANY] * len(after), out_specs_l, out_shape_l,
                   list(scratch_shapes), aliases, list(operands) + list(after))
        return res[0] if single else res
    assert not after
    n_in, n_rin = len(in_specs), len(rider.operands)
    n_out, n_rout = len(out_specs_l), len(rider.out_shapes)
    n_sc = len(scratch_shapes)

    def wrapped(*refs):
        main_in, r_in = refs[:n_in], refs[n_in:n_in + n_rin]
        o0 = n_in + n_rin
        main_out, r_out = refs[o0:o0 + n_out], refs[o0 + n_out:o0 + n_out + n_rout]
        s0 = o0 + n_out + n_rout
        main_sc, r_sc = refs[s0:s0 + n_sc], refs[s0 + n_sc:]
        ids = [pl.program_id(d) for d in range(len(grid))]
        first = functools.reduce(jnp.logical_and, [i == 0 for i in ids])
        last = functools.reduce(jnp.logical_and, [i == n - 1 for i, n in zip(ids, grid)])

        @pl.when(first)
        def _():
            rider.start(r_in, r_out, r_sc)

        body(*main_in, *main_out, *main_sc)

        @pl.when(last)
        def _():
            rider.wait(r_in, r_out, r_sc)

    for src, dst in rider.aliases.items():
        aliases[n_in + src] = n_out + dst
    res = call(wrapped, list(in_specs) + [ANY] * n_rin, out_specs_l + [ANY] * n_rout,
               out_shape_l + rider.out_shapes, list(scratch_shapes) + rider.scratch, aliases,
               list(operands) + rider.operands)
    main = res[:n_out]
    return (main[0] if single else main), res[n_out:]


def _matmul(a, b, *, mode, tm, tn, tk, out_dtype, name, b_blocked=False,
            out_blocked=None, cols_outer=False, rider=None, after=()):
    a_shape = a.shape
    if mode == "nn":
        m_dim, k_dim = a_shape
        n_dim = b.shape[0] * b.shape[2] if b_blocked else b.shape[1]
        rows, cols, red = m_dim, n_dim, k_dim
    elif mode == "nt":
        m_dim, n_dim = a_shape
        k_dim = b.shape[1] if b_blocked else b.shape[0]
        rows, cols, red = m_dim, k_dim, n_dim
    else:
        m_dim, k_dim = a_shape
        n_dim = b.shape[1]
        rows, cols, red = k_dim, n_dim, m_dim
    assert rows % tm == 0 and cols % tn == 0 and red % tk == 0, (name, rows, cols, red)
    ni, nj, nk = rows // tm, cols // tn, red // tk

    if mode == "nn":
        a_spec = pl.BlockSpec((tm, tk), lambda i, j, k: (i, k))
        if b_blocked:
            per = b.shape[2] // tn
            b_spec = pl.BlockSpec((None, tk, tn), lambda i, j, k: (j // per, k, j % per))
        else:
            b_spec = pl.BlockSpec((tk, tn), lambda i, j, k: (k, j))
        dims = (((1,), (0,)), ((), ()))
    elif mode == "nt":
        a_spec = pl.BlockSpec((tm, tk), lambda i, j, k: (i, k))
        if b_blocked:
            per = b.shape[2] // tk
            b_spec = pl.BlockSpec((None, tn, tk), lambda i, j, k: (k // per, j, k % per))
        else:
            b_spec = pl.BlockSpec((tn, tk), lambda i, j, k: (j, k))
        dims = (((1,), (1,)), ((), ()))
    else:
        a_spec = pl.BlockSpec((tk, tm), lambda i, j, k: (k, i))
        b_spec = pl.BlockSpec((tk, tn), lambda i, j, k: (k, j))
        dims = (((0,), (0,)), ((), ()))

    if out_blocked:
        per_o = (cols // out_blocked) // tn
        out_spec = pl.BlockSpec((None, tm, tn), lambda i, j, k: (j // per_o, i, j % per_o))
        out_shape = jax.ShapeDtypeStruct((out_blocked, rows, cols // out_blocked), out_dtype)
    else:
        out_spec = pl.BlockSpec((tm, tn), lambda i, j, k: (i, j))
        out_shape = jax.ShapeDtypeStruct((rows, cols), out_dtype)

    def body(a_ref, b_ref, o_ref, *acc):
        prod = lax.dot_general(a_ref[...], b_ref[...], dims, preferred_element_type=F32)
        if nk == 1:
            o_ref[...] = prod.astype(out_dtype)
        else:
            acc_ref, = acc
            k = pl.program_id(2)

            @pl.when(k == 0)
            def _():
                acc_ref[...] = prod

            @pl.when(k > 0)
            def _():
                acc_ref[...] += prod

            @pl.when(k == nk - 1)
            def _():
                o_ref[...] = acc_ref[...].astype(out_dtype)

    scratch = [pltpu.VMEM((tm, tn), F32)] if nk > 1 else []
    grid = (ni, nj, nk)
    if cols_outer:
        swap = lambda spec: pl.BlockSpec(spec.block_shape, lambda j, i, k, f=spec.index_map: f(i, j, k))
        a_spec, b_spec, out_spec, grid = swap(a_spec), swap(b_spec), swap(out_spec), (nj, ni, nk)
    return _pallas(body, name=name, grid=grid, in_specs=[a_spec, b_spec], out_specs=out_spec,
                   out_shape=out_shape, operands=[a, b], scratch_shapes=scratch, rider=rider, after=after)


def _proj_by_slot(h, w_in, order, first, count, name, proj=None, after=()):
    s, k = h.shape
    nb = w_in.shape[2]
    tm = 512
    prev = [] if proj is None else [proj]

    def body(order_ref, h_ref, w_ref, *refs):
        refs[-1][...] = jnp.dot(h_ref[...], w_ref[...], preferred_element_type=F32)

    return pl.pallas_call(
        body, name=name,
        grid_spec=pltpu.PrefetchScalarGridSpec(
            num_scalar_prefetch=1, grid=(count, s // tm),
            in_specs=[pl.BlockSpec((tm, k), lambda j, i, order_ref: (i, 0)),
                      pl.BlockSpec((None, k, nb), lambda j, i, order_ref: (order_ref[first + j], 0, 0))]
            + [ANY] * (len(prev) + len(after)),
            out_specs=pl.BlockSpec((tm, nb), lambda j, i, order_ref: (i, order_ref[first + j]))),
        out_shape=jax.ShapeDtypeStruct((s, N_CHIPS * nb), F32),
        input_output_aliases={3: 0} if prev else {}, compiler_params=_params(),
    )(order, h, w_in, *prev, *after)


def _rmsnorm_fwd(x, w, name):
    s, d = x.shape
    tm = 256

    def body(x_ref, w_ref, o_ref):
        xv = x_ref[...]
        rstd = lax.rsqrt(jnp.mean(xv * xv, axis=-1, keepdims=True) + EPS)
        o_ref[...] = (xv * rstd * w_ref[...]).astype(BF16)

    return pl.pallas_call(
        body, name=name, grid=(s // tm,),
        in_specs=[pl.BlockSpec((tm, d), lambda i: (i, 0)), pl.BlockSpec((1, d), lambda i: (0, 0))],
        out_specs=pl.BlockSpec((tm, d), lambda i: (i, 0)),
        out_shape=jax.ShapeDtypeStruct((s, d), BF16), compiler_params=_params(),
    )(x, w)


def _matmul_nt_rmsnorm_bwd(dz, w_blocked, x, w, dres, name, rider=None, after=()):
    s, d = x.shape
    nk, _, nb = w_blocked.shape
    tm = 512
    nt_dims = (((1,), (1,)), ((), ()))

    def body(a_ref, b_ref, x_ref, w_ref, dres_ref, dx_ref, dxb_ref, dw_ref, acc_ref):
        k, i = pl.program_id(0), pl.program_id(1)
        rows = pl.ds(pl.multiple_of(i * tm, tm), tm)
        prod = lax.dot_general(a_ref[...], b_ref[...], nt_dims, preferred_element_type=F32)

        @pl.when(k == 0)
        def _():
            acc_ref[rows, :] = prod

        @pl.when(jnp.logical_and(k > 0, k < nk - 1))
        def _():
            acc_ref[rows, :] += prod

        @pl.when(k == nk - 1)
        def _():
            xv = x_ref[...]
            rstd = lax.rsqrt(jnp.mean(xv * xv, axis=-1, keepdims=True) + EPS)
            xhat = xv * rstd
            dhv = acc_ref[rows, :] + prod
            g = dhv * w_ref[...]
            dx = rstd * (g - xhat * jnp.mean(g * xhat, axis=-1, keepdims=True)) + dres_ref[...]
            dx_ref[...] = dx
            dxb_ref[...] = dx.astype(BF16)
            part = jnp.sum(dhv * xhat, axis=0, keepdims=True)

            @pl.when(i == 0)
            def _():
                dw_ref[...] = part

            @pl.when(i > 0)
            def _():
                dw_ref[...] += part

    assert nk >= 2
    row = pl.BlockSpec((tm, d), lambda k, i: (jnp.where(k == nk - 1, i, 0), 0))
    vec = pl.BlockSpec((1, d), lambda k, i: (0, 0))
    return _pallas(
        body, name=name, grid=(nk, s // tm),
        in_specs=[pl.BlockSpec((tm, nb), lambda k, i: (i, k)), pl.BlockSpec((None, d, nb), lambda k, i: (k, 0, 0)),
                  row, vec, row],
        out_specs=[row, row, vec],
        out_shape=[jax.ShapeDtypeStruct((s, d), F32), jax.ShapeDtypeStruct((s, d), BF16),
                   jax.ShapeDtypeStruct((1, d), F32)],
        operands=[dz, w_blocked, x, w, dres], scratch_shapes=[pltpu.VMEM((s, d), F32)], rider=rider, after=after)


def _rope_consts():
    lane = np.arange(LANES)
    in_head = lane % HEAD_DIM
    inv_freq = ROPE_THETA ** (-jnp.arange(0, ROT_DIM, 2, dtype=F32) / ROT_DIM)
    invf = jnp.where(jnp.asarray(in_head < ROT_DIM), jnp.tile(inv_freq, LANES // (ROT_DIM // 2)), 0.0)
    m_a = np.where(in_head < ROT_DIM // 2, -1.0, 0.0).astype(np.float32)
    m_b = np.where((in_head >= ROT_DIM // 2) & (in_head < ROT_DIM), 1.0, 0.0).astype(np.float32)
    block_diag = (lane[:, None] // HEAD_DIM == lane[None, :] // HEAD_DIM).astype(np.float32)
    return (invf.reshape(1, LANES).astype(F32), jnp.asarray(m_a).reshape(1, LANES),
            jnp.asarray(m_b).reshape(1, LANES), jnp.asarray(block_diag, dtype=BF16))


def _head_sums(v, bd):
    hi = v.astype(BF16)
    lo = (v - hi.astype(F32)).astype(BF16)
    return jnp.dot(hi, bd, preferred_element_type=F32) + jnp.dot(lo, bd, preferred_element_type=F32)


def _qk_fwd(proj, pos_col, qw2, kw2, consts, name, rider=None):
    s = proj.shape[0]
    width = 3 * N_SLOT_HEADS * HEAD_DIM
    tm = 128
    invf, m_a, m_b, bd = consts
    scale = HEAD_DIM ** -0.5

    def body(q_ref, k_ref, pos_ref, qw_ref, kw_ref, invf_ref, ma_ref, mb_ref, bd_ref, qo_ref, ko_ref):
        ang = pos_ref[...].astype(F32) * invf_ref[...]
        cos = jnp.cos(ang)
        sin = jnp.sin(ang)
        s_a = sin * ma_ref[...]
        s_b = sin * mb_ref[...]
        bdv = bd_ref[...]
        for src, w_ref, dst, sc in ((q_ref, qw_ref, qo_ref, scale), (k_ref, kw_ref, ko_ref, 1.0)):
            for cb in range(width // LANES):
                cols = slice(cb * LANES, (cb + 1) * LANES)
                t = src[:, cols]
                rstd = lax.rsqrt(_head_sums(t * t, bdv) * (1.0 / HEAD_DIM) + EPS)
                y = t * rstd * w_ref[...]
                r = y * cos + pltpu.roll(y, LANES - 8, axis=1) * s_a + pltpu.roll(y, 8, axis=1) * s_b
                dst[:, cols] = r * sc if sc != 1.0 else r

    vec = pl.BlockSpec((1, LANES), lambda i: (0, 0))
    return _pallas(
        body, name=name, grid=(s // tm,),
        in_specs=[pl.BlockSpec((tm, width), lambda i: (i, 0)), pl.BlockSpec((tm, width), lambda i: (i, 1)),
                  pl.BlockSpec((tm, 1), lambda i: (i, 0)), vec, vec, vec, vec, vec,
                  pl.BlockSpec((LANES, LANES), lambda i: (0, 0))],
        out_specs=[pl.BlockSpec((tm, width), lambda i: (i, 0))] * 2,
        out_shape=[jax.ShapeDtypeStruct((s, width), F32)] * 2,
        operands=[proj, proj, pos_col, qw2, kw2, invf, m_a, m_b, bd], rider=rider)


def _qk_bwd(dqn, dkn, dv, da, db, dgl, proj, pos_col, qw2, kw2, consts, name, rider=None):
    s = proj.shape[0]
    width = 3 * N_SLOT_HEADS * HEAD_DIM
    ch = da.shape[1]
    gate_w = dgl.shape[1]
    out_w = 3 * width + 2 * ch + gate_w
    assert out_w == proj.shape[1]
    tm = 128
    invf, m_a, m_b, bd = consts
    scale = HEAD_DIM ** -0.5

    def body(dq_ref, dk_ref, dv_ref, da_ref, db_ref, dgl_ref, q_ref, k_ref, pos_ref, qw_ref, kw_ref,
             invf_ref, ma_ref, mb_ref, bd_ref, out_ref, dqw_ref, dkw_ref):
        ang = pos_ref[...].astype(F32) * invf_ref[...]
        cos = jnp.cos(ang)
        sin = jnp.sin(ang)
        s_a = sin * ma_ref[...]
        s_b = sin * mb_ref[...]
        bdv = bd_ref[...]
        first = pl.program_id(0) == 0
        for src, dsrc, w_ref, col0, dw_ref, sc in ((q_ref, dq_ref, qw_ref, 0, dqw_ref, scale),
                                                   (k_ref, dk_ref, kw_ref, width, dkw_ref, 1.0)):
            dw_acc = jnp.zeros((1, LANES), F32)
            for cb in range(width // LANES):
                cols = slice(cb * LANES, (cb + 1) * LANES)
                t = src[:, cols]
                dr = dsrc[:, cols]
                if sc != 1.0:
                    dr = dr * sc
                dy = dr * cos + pltpu.roll(dr * s_a, 8, axis=1) + pltpu.roll(dr * s_b, LANES - 8, axis=1)
                rstd = lax.rsqrt(_head_sums(t * t, bdv) * (1.0 / HEAD_DIM) + EPS)
                xhat = t * rstd
                g = dy * w_ref[...]
                dt = rstd * (g - xhat * (_head_sums(g * xhat, bdv) * (1.0 / HEAD_DIM)))
                out_ref[:, col0 + cb * LANES: col0 + (cb + 1) * LANES] = dt.astype(BF16)
                dw_acc = dw_acc + jnp.sum(dy * xhat, axis=0, keepdims=True)
            dw_acc = dw_acc + pltpu.roll(dw_acc, HEAD_DIM, axis=1)

            @pl.when(first)
            def _(dw_ref=dw_ref, dw_acc=dw_acc):
                dw_ref[...] = dw_acc

            @pl.when(jnp.logical_not(first))
            def _(dw_ref=dw_ref, dw_acc=dw_acc):
                dw_ref[...] += dw_acc
        out_ref[:, 2 * width: 3 * width] = dv_ref[...].astype(BF16)
        out_ref[:, 3 * width: 3 * width + ch] = da_ref[...]
        out_ref[:, 3 * width + ch: 3 * width + 2 * ch] = db_ref[...]
        out_ref[:, 3 * width + 2 * ch: out_w] = dgl_ref[...]

    vec = pl.BlockSpec((1, LANES), lambda i: (0, 0))
    blk = lambda c: pl.BlockSpec((tm, width), lambda i: (i, c))
    cblk = pl.BlockSpec((tm, ch), lambda i: (i, 0))
    return _pallas(
        body, name=name, grid=(s // tm,),
        in_specs=[blk(0), blk(0), blk(0), cblk, cblk, pl.BlockSpec((tm, gate_w), lambda i: (i, 0)),
                  blk(0), blk(1), pl.BlockSpec((tm, 1), lambda i: (i, 0)), vec, vec, vec, vec, vec,
                  pl.BlockSpec((LANES, LANES), lambda i: (0, 0))],
        out_specs=[pl.BlockSpec((tm, out_w), lambda i: (i, 0)), vec, vec],
        out_shape=[jax.ShapeDtypeStruct((s, out_w), BF16)] + [jax.ShapeDtypeStruct((1, LANES), F32)] * 2,
        operands=[dqn, dkn, dv, da, db, dgl, proj, proj, pos_col, qw2, kw2, invf, m_a, m_b, bd],
        rider=rider)


def _row_chunks(n_rows, fn, chunk=256):
    def step(i, c):
        fn(pl.ds(pl.multiple_of(i * chunk, chunk), chunk))
        return c
    lax.fori_loop(0, n_rows // chunk, step, 0)


def _to_residue_major(dst, src, s, d, dst_off=0, cast=None):
    seq = s // d
    for r in range(d):
        v = src[...] if d == 1 else src[pl.ds(r, seq, stride=d), :]
        dst[dst_off + r * seq: dst_off + (r + 1) * seq, :] = v if cast is None else v.astype(cast)


def _from_residue_major(dst, src, s, d, src_off=0):
    seq = s // d
    for r in range(d):
        v = src[src_off + r * seq: src_off + (r + 1) * seq, :]
        if d == 1:
            dst[...] = v
        else:
            dst[pl.ds(r, seq, stride=d), :] = v


def _band_bias():
    qi = lax.broadcasted_iota(jnp.int32, (QBLK, KWIN), 0)
    kj = lax.broadcasted_iota(jnp.int32, (QBLK, KWIN), 1)
    return jnp.where(jnp.abs(kj - HALF_SPAN - qi) <= HALF_SPAN, 0.0, NEG_INF).astype(F32)


def _range_bias(base, seq):
    kj = lax.broadcasted_iota(jnp.int32, (1, KWIN), 1)
    lo = (base & -seq) - base + HALF_SPAN
    return jnp.where((kj >= lo) & (kj < lo + seq), 0.0, NEG_INF).astype(F32)


def _skewed_blocks(n_blk, produce, consume):
    produce(0, 0)
    for b in range(n_blk):
        consume(b, b % 2)
        if b + 1 < n_blk:
            produce(b + 1, (b + 1) % 2)


def _block_base(b):
    return b * QBLK if isinstance(b, int) else pl.multiple_of(b * QBLK, QBLK)


def _attn_fwd(qn, kn, proj, name, rider=None):
    s = qn.shape[0]
    n_pairs = N_SLOT_HEADS * HEAD_DIM // LANES
    v_col0 = 2 * qn.shape[1] // LANES
    nt_dims = (((1,), (1,)), ((), ()))

    def body(q_ref, k_ref, v_ref, attn_ref, lse_ref, attn_b_ref, q_rm, k_rm, v_rm, acc_rm, m_rm, l_rm,
             acc_p, m_p, l_p, m_run, l_run, acc_run, band, s_buf, m_buf):
        g = pl.program_id(1)
        zpad = jnp.zeros((HALF_SPAN, LANES), BF16)
        k_rm[0:HALF_SPAN, :] = zpad
        k_rm[s + HALF_SPAN: s + 2 * HALF_SPAN, :] = zpad
        v_rm[0:HALF_SPAN, 0:LANES] = zpad
        v_rm[s + HALF_SPAN: s + 2 * HALF_SPAN, 0:LANES] = zpad

        def ones_rows(rows):
            v_rm[pl.ds(rows.start, rows.size), LANES:2 * LANES] = jnp.ones((rows.size, LANES), BF16)

        _row_chunks(s + 2 * HALF_SPAN, ones_rows, chunk=2 * HALF_SPAN)
        band[...] = _band_bias()
        lane = lax.broadcasted_iota(jnp.int32, (QBLK, LANES), 1)
        low = lane < HEAD_DIM
        n_blk = s // QBLK

        for gi, d in enumerate(DILATIONS):
            @pl.when(g == gi)
            def _(gi=gi, d=d):
                seq = s // d
                _to_residue_major(q_rm, q_ref, s, d, cast=BF16)
                _to_residue_major(k_rm, k_ref, s, d, dst_off=HALF_SPAN, cast=BF16)
                _to_residue_major(v_rm.at[:, 0:LANES], v_ref, s, d, dst_off=HALF_SPAN, cast=BF16)

                def scores(b, slot):
                    base = _block_base(b)
                    q = q_rm[pl.ds(base, QBLK), :]
                    zero = jnp.zeros_like(q)
                    q2 = jnp.concatenate([jnp.where(low, q, zero), jnp.where(low, zero, q)], axis=0)
                    sc = lax.dot_general(q2, k_rm[pl.ds(base, KWIN), :], nt_dims, preferred_element_type=F32)
                    bias = band[...] + _range_bias(base, seq)
                    for hh in range(2):
                        rows = slice(hh * QBLK, (hh + 1) * QBLK)
                        sh = sc[rows, :] + bias
                        s_buf[slot, rows, :] = sh
                        m_buf[slot, rows, :] = jnp.broadcast_to(jnp.max(sh, axis=-1, keepdims=True), (QBLK, LANES))

                def outputs(b, slot):
                    base = _block_base(b)
                    sv = s_buf[slot]
                    mb = m_buf[slot]
                    p = jnp.exp(jnp.concatenate([sv[:, 0:LANES] - mb, sv[:, LANES:2 * LANES] - mb], axis=1))
                    pv = jnp.dot(p.astype(BF16), v_rm[pl.ds(base, KWIN), :], preferred_element_type=F32)
                    rows = pl.ds(base, QBLK)
                    acc_rm[rows, :] = jnp.where(low, pv[0:QBLK, 0:LANES], pv[QBLK:2 * QBLK, 0:LANES])
                    l_rm[rows, :] = jnp.where(low, pv[0:QBLK, LANES:2 * LANES], pv[QBLK:2 * QBLK, LANES:2 * LANES])
                    m_rm[rows, :] = jnp.where(low, mb[0:QBLK, :], mb[QBLK:2 * QBLK, :])

                _skewed_blocks(n_blk, scores, outputs)
                if d == 1:
                    src = (acc_rm, m_rm, l_rm)
                else:
                    for dst_, src_ in ((acc_p, acc_rm), (m_p, m_rm), (l_p, l_rm)):
                        _from_residue_major(dst_, src_, s, d)
                    src = (acc_p, m_p, l_p)

                def combine(rows):
                    a_g, m_g, l_g = src[0][rows, :], src[1][rows, :], src[2][rows, :]
                    if gi == 0:
                        m_new, l_new, a_new = m_g, l_g, a_g
                    else:
                        m_old = m_run[rows, :]
                        m_new = jnp.maximum(m_old, m_g)
                        w_old = jnp.exp(m_old - m_new)
                        w_g = jnp.exp(m_g - m_new)
                        l_new = l_run[rows, :] * w_old + l_g * w_g
                        a_new = acc_run[rows, :] * w_old + a_g * w_g
                    if gi == len(DILATIONS) - 1:
                        out = a_new / l_new
                        attn_ref[rows, :] = out
                        attn_b_ref[rows, :] = out.astype(BF16)
                        lse_ref[rows, :] = m_new + jnp.log(l_new)
                    else:
                        m_run[rows, :] = m_new
                        l_run[rows, :] = l_new
                        acc_run[rows, :] = a_new

                _row_chunks(s, combine)

    qk_spec = pl.BlockSpec((s, LANES), lambda hp, g: (0, g * n_pairs + hp))
    v_spec = pl.BlockSpec((s, LANES), lambda hp, g: (0, v_col0 + g * n_pairs + hp))
    o_spec = pl.BlockSpec((s, LANES), lambda hp, g: (0, hp))
    f32buf = pltpu.VMEM((s, LANES), F32)
    return _pallas(
        body, name=name, grid=(n_pairs, len(DILATIONS)), in_specs=[qk_spec, qk_spec, v_spec],
        out_specs=[o_spec, o_spec, o_spec],
        out_shape=[jax.ShapeDtypeStruct((s, n_pairs * LANES), F32)] * 2
        + [jax.ShapeDtypeStruct((s, n_pairs * LANES), BF16)],
        operands=[qn, kn, proj],
        scratch_shapes=[pltpu.VMEM((s, LANES), BF16), pltpu.VMEM((s + 2 * HALF_SPAN, LANES), BF16),
                        pltpu.VMEM((s + 2 * HALF_SPAN, 2 * LANES), BF16)] + [f32buf] * 9
        + [pltpu.VMEM((QBLK, KWIN), F32), pltpu.VMEM((2, 2 * QBLK, KWIN), F32),
           pltpu.VMEM((2, 2 * QBLK, LANES), F32)],
        rider=rider)


def _attn_bwd(qn, kn, proj, dattn, attn, lse, bd, name, rider=None):
    s = qn.shape[0]
    n_pairs = N_SLOT_HEADS * HEAD_DIM // LANES
    v_col0 = 2 * qn.shape[1] // LANES
    nt_dims = (((1,), (1,)), ((), ()))
    tn_dims = (((0,), (0,)), ((), ()))
    spad = s + 2 * HALF_SPAN

    def body(q_ref, k_ref, v_ref, do_ref, o_ref, lse_ref, bd_ref, dq_ref, dk_ref, dv_ref,
             q_rm, k_rm, v_rm, do_rm, lse0_rm, lse1_rm, dd0_rm, dd1_rm, dq_rm, dk_rm, dv_rm,
             lse0_p, lse1_p, dd0_p, dd1_p, band, p_buf, ds_buf):
        g = pl.program_id(1)
        zpad = jnp.zeros((HALF_SPAN, LANES), BF16)
        for buf in (k_rm, v_rm):
            buf[0:HALF_SPAN, :] = zpad
            buf[s + HALF_SPAN: spad, :] = zpad
        zf = jnp.zeros((HALF_SPAN, LANES), F32)
        for buf in (dk_rm, dv_rm):
            buf[0:HALF_SPAN, :] = zf
            buf[s + HALF_SPAN: spad, :] = zf
        band[...] = _band_bias()

        def clear(rows):
            z = jnp.zeros((rows.size, LANES), F32)
            dk_rm[pl.ds(rows.start + HALF_SPAN, rows.size), :] = z
            dv_rm[pl.ds(rows.start + HALF_SPAN, rows.size), :] = z

        _row_chunks(s, clear)

        def prepare(rows):
            lo = lax.broadcasted_iota(jnp.int32, (rows.size, LANES), 1) < HEAD_DIM
            dsum = _head_sums(do_ref[rows, :] * o_ref[rows, :], bd_ref[...])
            dswap = pltpu.roll(dsum, HEAD_DIM, axis=1)
            dd0_p[rows, :] = jnp.where(lo, dsum, dswap)
            dd1_p[rows, :] = jnp.where(lo, dswap, dsum)
            lv = lse_ref[rows, :]
            lswap = pltpu.roll(lv, HEAD_DIM, axis=1)
            lse0_p[rows, :] = jnp.where(lo, lv, lswap)
            lse1_p[rows, :] = jnp.where(lo, lswap, lv)

        @pl.when(g == 0)
        def _():
            _row_chunks(s, prepare)
        lane = lax.broadcasted_iota(jnp.int32, (QBLK, LANES), 1)
        low = lane < HEAD_DIM
        n_blk = s // QBLK

        def stacked(ref, rows):
            val = ref[rows, :]
            zero = jnp.zeros_like(val)
            return jnp.concatenate([jnp.where(low, val, zero), jnp.where(low, zero, val)], axis=0)

        for gi, d in enumerate(DILATIONS):
            @pl.when(g == gi)
            def _(d=d):
                seq = s // d
                _to_residue_major(q_rm, q_ref, s, d, cast=BF16)
                _to_residue_major(k_rm, k_ref, s, d, dst_off=HALF_SPAN, cast=BF16)
                _to_residue_major(v_rm, v_ref, s, d, dst_off=HALF_SPAN, cast=BF16)
                _to_residue_major(do_rm, do_ref, s, d, cast=BF16)
                for dst_, src_ in ((lse0_rm, lse0_p), (lse1_rm, lse1_p), (dd0_rm, dd0_p), (dd1_rm, dd1_p)):
                    _to_residue_major(dst_, src_, s, d)

                def scores(b, slot):
                    base = _block_base(b)
                    rows = pl.ds(base, QBLK)
                    win = pl.ds(base, KWIN)
                    sc = lax.dot_general(stacked(q_rm, rows), k_rm[win, :], nt_dims, preferred_element_type=F32)
                    dp = lax.dot_general(stacked(do_rm, rows), v_rm[win, :], nt_dims, preferred_element_type=F32)
                    bias = band[...] + _range_bias(base, seq)
                    for hh, (lse_r, dd_r) in enumerate(((lse0_rm, dd0_rm), (lse1_rm, dd1_rm))):
                        r = slice(hh * QBLK, (hh + 1) * QBLK)
                        lse_h = lse_r[rows, :]
                        dd_h = dd_r[rows, :]
                        sh = sc[r, :] + bias
                        p = jnp.exp(jnp.concatenate([sh[:, 0:LANES] - lse_h, sh[:, LANES:KWIN] - lse_h], axis=1))
                        dph = dp[r, :]
                        ds = p * jnp.concatenate([dph[:, 0:LANES] - dd_h, dph[:, LANES:KWIN] - dd_h], axis=1)
                        p_buf[slot, r, :] = p.astype(BF16)
                        ds_buf[slot, r, :] = ds.astype(BF16)

                def grads(b, slot):
                    base = _block_base(b)
                    rows = pl.ds(base, QBLK)
                    win = pl.ds(base, KWIN)
                    p = p_buf[slot]
                    ds = ds_buf[slot]
                    dq2 = jnp.dot(ds, k_rm[win, :], preferred_element_type=F32)
                    dq_rm[rows, :] = jnp.where(low, dq2[0:QBLK, :], dq2[QBLK:2 * QBLK, :])
                    dk_rm[win, :] += lax.dot_general(ds, stacked(q_rm, rows), tn_dims, preferred_element_type=F32)
                    dv_rm[win, :] += lax.dot_general(p, stacked(do_rm, rows), tn_dims, preferred_element_type=F32)

                _skewed_blocks(n_blk, scores, grads)
                _from_residue_major(dq_ref, dq_rm, s, d)
                _from_residue_major(dk_ref, dk_rm, s, d, src_off=HALF_SPAN)
                _from_residue_major(dv_ref, dv_rm, s, d, src_off=HALF_SPAN)

    qk_spec = pl.BlockSpec((s, LANES), lambda hp, g: (0, g * n_pairs + hp))
    v_spec = pl.BlockSpec((s, LANES), lambda hp, g: (0, v_col0 + g * n_pairs + hp))
    o_spec = pl.BlockSpec((s, LANES), lambda hp, g: (0, hp))
    width = qn.shape[1]
    f32buf = pltpu.VMEM((s, LANES), F32)
    f32pad = pltpu.VMEM((spad, LANES), F32)
    return _pallas(
        body, name=name, grid=(n_pairs, len(DILATIONS)),
        in_specs=[qk_spec, qk_spec, v_spec, o_spec, o_spec, o_spec,
                  pl.BlockSpec((LANES, LANES), lambda hp, g: (0, 0))],
        out_specs=[qk_spec, qk_spec, qk_spec],
        out_shape=[jax.ShapeDtypeStruct((s, width), F32)] * 3,
        operands=[qn, kn, proj, dattn, attn, lse, bd],
        scratch_shapes=[pltpu.VMEM((s, LANES), BF16), pltpu.VMEM((spad, LANES), BF16),
                        pltpu.VMEM((spad, LANES), BF16), pltpu.VMEM((s, LANES), BF16),
                        f32buf, f32buf, f32buf, f32buf, f32buf, f32pad, f32pad,
                        f32buf, f32buf, f32buf, f32buf, pltpu.VMEM((QBLK, KWIN), F32),
                        pltpu.VMEM((2, 2 * QBLK, KWIN), BF16), pltpu.VMEM((2, 2 * QBLK, KWIN), BF16)],
        rider=rider)


CONV_PAD = 16


def _conv_fwd(proj, conv_w, conv_b, col0, name, rider=None):
    s = proj.shape[0]
    ch = conv_w.shape[1]
    nblk = ch // LANES
    a0 = col0 // LANES
    tr = 256
    shift = CONV_PAD - (CONV_WIDTH - 1) // 2

    def body(a_ref, b_ref, w_ref, bias_ref, u0_ref, uc_ref, pad):
        z = jnp.zeros((CONV_PAD, LANES), F32)
        pad[0:CONV_PAD, :] = z
        pad[s + CONV_PAD: s + 2 * CONV_PAD, :] = z

        def glu(rows):
            u0 = a_ref[rows, :] * jax.nn.sigmoid(b_ref[rows, :])
            u0_ref[rows, :] = u0
            pad[pl.ds(rows.start + CONV_PAD, rows.size), :] = u0

        _row_chunks(s, glu)
        for t in range(0, s, tr):
            acc = jnp.broadcast_to(bias_ref[...], (tr, LANES))
            for k in range(CONV_WIDTH):
                acc = acc + w_ref[k:k + 1, :] * pad[t + k + shift: t + k + shift + tr, :]
            uc_ref[t:t + tr, :] = acc

    return _pallas(
        body, name=name, grid=(nblk,),
        in_specs=[pl.BlockSpec((s, LANES), lambda c: (0, a0 + c)),
                  pl.BlockSpec((s, LANES), lambda c: (0, a0 + nblk + c)),
                  pl.BlockSpec((CONV_WIDTH, LANES), lambda c: (0, c)),
                  pl.BlockSpec((1, LANES), lambda c: (0, c))],
        out_specs=[pl.BlockSpec((s, LANES), lambda c: (0, c))] * 2,
        out_shape=[jax.ShapeDtypeStruct((s, ch), F32)] * 2, operands=[proj, proj, conv_w, conv_b],
        scratch_shapes=[pltpu.VMEM((s + 2 * CONV_PAD, LANES), F32)], rider=rider)


def _ln_silu_fwd(uc, ln_w, ln_b, name):
    s, ch = uc.shape
    tm = 256

    def body(u_ref, w_ref, b_ref, o_ref):
        u = u_ref[...]
        mu = jnp.mean(u, axis=-1, keepdims=True)
        xc = u - mu
        rstd = lax.rsqrt(jnp.mean(xc * xc, axis=-1, keepdims=True) + EPS)
        z = xc * rstd * w_ref[...] + b_ref[...]
        o_ref[...] = (z * jax.nn.sigmoid(z)).astype(BF16)

    row = pl.BlockSpec((tm, ch), lambda i: (i, 0))
    vec = pl.BlockSpec((1, ch), lambda i: (0, 0))
    return pl.pallas_call(
        body, name=name, grid=(s // tm,), in_specs=[row, vec, vec], out_specs=row,
        out_shape=jax.ShapeDtypeStruct((s, ch), BF16), compiler_params=_params(),
    )(uc, ln_w, ln_b)


def _ln_silu_bwd(du3, uc, ln_w, ln_b, name):
    s, ch = uc.shape
    tm = 256

    def body(d_ref, u_ref, w_ref, b_ref, du_ref, dw_ref, db_ref):
        u = u_ref[...]
        mu = jnp.mean(u, axis=-1, keepdims=True)
        xc = u - mu
        rstd = lax.rsqrt(jnp.mean(xc * xc, axis=-1, keepdims=True) + EPS)
        xhat = xc * rstd
        z = xhat * w_ref[...] + b_ref[...]
        sg = jax.nn.sigmoid(z)
        dz = d_ref[...] * (sg * (1.0 + z * (1.0 - sg)))
        dxh = dz * w_ref[...]
        du_ref[...] = rstd * (dxh - jnp.mean(dxh, axis=-1, keepdims=True)
                              - xhat * jnp.mean(dxh * xhat, axis=-1, keepdims=True))
        pw = jnp.sum(dz * xhat, axis=0, keepdims=True)
        pb = jnp.sum(dz, axis=0, keepdims=True)
        first = pl.program_id(0) == 0

        @pl.when(first)
        def _():
            dw_ref[...] = pw
            db_ref[...] = pb

        @pl.when(jnp.logical_not(first))
        def _():
            dw_ref[...] += pw
            db_ref[...] += pb

    row = pl.BlockSpec((tm, ch), lambda i: (i, 0))
    vec = pl.BlockSpec((1, ch), lambda i: (0, 0))
    return pl.pallas_call(
        body, name=name, grid=(s // tm,), in_specs=[row, row, vec, vec], out_specs=[row, vec, vec],
        out_shape=[jax.ShapeDtypeStruct((s, ch), F32), jax.ShapeDtypeStruct((1, ch), F32),
                   jax.ShapeDtypeStruct((1, ch), F32)],
        compiler_params=_params(),
    )(du3, uc, ln_w, ln_b)


def _conv_bwd(duc, u0, proj, conv_w, col0, name, rider=None):
    s = proj.shape[0]
    ch = conv_w.shape[1]
    nblk = ch // LANES
    a0 = col0 // LANES
    tr = 256
    half = (CONV_WIDTH - 1) // 2
    shift = CONV_PAD - half

    def body(duc_ref, u0_ref, a_ref, b_ref, w_ref, da_ref, db_ref, dw_ref, dbias_ref, pad_d, pad_u):
        z = jnp.zeros((CONV_PAD, LANES), F32)
        for buf in (pad_d, pad_u):
            buf[0:CONV_PAD, :] = z
            buf[s + CONV_PAD: s + 2 * CONV_PAD, :] = z

        def fill(rows):
            dst = pl.ds(rows.start + CONV_PAD, rows.size)
            pad_d[dst, :] = duc_ref[rows, :]
            pad_u[dst, :] = u0_ref[rows, :]

        _row_chunks(s, fill)
        dw_acc = [jnp.zeros((8, LANES), F32) for _ in range(CONV_WIDTH)]
        dbias_acc = jnp.zeros((8, LANES), F32)
        for t in range(0, s, tr):
            d_t = duc_ref[t:t + tr, :]
            dbias_acc = dbias_acc + jnp.sum(d_t.reshape(tr // 8, 8, LANES), axis=0)
            du0 = jnp.zeros((tr, LANES), F32)
            for k in range(CONV_WIDTH):
                du0 = du0 + w_ref[k:k + 1, :] * pad_d[t - k + half + CONV_PAD: t - k + half + CONV_PAD + tr, :]
                prod = d_t * pad_u[t + k + shift: t + k + shift + tr, :]
                dw_acc[k] = dw_acc[k] + jnp.sum(prod.reshape(tr // 8, 8, LANES), axis=0)
            av = a_ref[t:t + tr, :]
            sg = jax.nn.sigmoid(b_ref[t:t + tr, :])
            da_ref[t:t + tr, :] = (du0 * sg).astype(BF16)
            db_ref[t:t + tr, :] = (du0 * av * sg * (1.0 - sg)).astype(BF16)
        for k in range(CONV_WIDTH):
            dw_ref[k:k + 1, :] = jnp.sum(dw_acc[k], axis=0, keepdims=True)
        dbias_ref[...] = jnp.sum(dbias_acc, axis=0, keepdims=True)

    col = lambda off: pl.BlockSpec((s, LANES), lambda c: (0, off + c))
    return _pallas(
        body, name=name, grid=(nblk,),
        in_specs=[col(0), col(0), col(a0), col(a0 + nblk),
                  pl.BlockSpec((CONV_WIDTH, LANES), lambda c: (0, c))],
        out_specs=[col(0), col(0), pl.BlockSpec((CONV_WIDTH, LANES), lambda c: (0, c)),
                   pl.BlockSpec((1, LANES), lambda c: (0, c))],
        out_shape=[jax.ShapeDtypeStruct((s, ch), BF16)] * 2
        + [jax.ShapeDtypeStruct((CONV_WIDTH, ch), F32), jax.ShapeDtypeStruct((1, ch), F32)],
        operands=[duc, u0, proj, proj, conv_w],
        scratch_shapes=[pltpu.VMEM((s + 2 * CONV_PAD, LANES), F32)] * 2, rider=rider)


def _mix_out_proj(attn_b, u3, w_o, w_pw, proj, bg, col0, w_out, x, norm_w, name, rider=None):
    s, d = x.shape
    k = attn_b.shape[1]
    tm = 256
    half = d // 2
    assert col0 % half == 0
    c0 = col0 // half

    def body(a_ref, u_ref, wo_ref, wp_ref, a0_ref, a1_ref, b0_ref, b1_ref, bias_ref, w_ref, x_ref, nw_ref,
             ya_ref, yb_ref, mixed_ref, x1_ref, h2_ref):
        mixed = None
        for br, (src_ref, wb_ref, lo_ref, hi_ref, y_ref) in enumerate(((a_ref, wo_ref, a0_ref, a1_ref, ya_ref),
                                                                       (u_ref, wp_ref, b0_ref, b1_ref, yb_ref))):
            src = src_ref[...]
            y = jnp.concatenate([jnp.dot(src, wb_ref[j], preferred_element_type=F32) for j in range(N_CHIPS)],
                                axis=1)
            y_ref[...] = y
            logits = jnp.concatenate([lo_ref[...], hi_ref[...]], axis=1)
            part = jax.nn.sigmoid(logits + bias_ref[br]) * y
            mixed = part if mixed is None else mixed + part
        mixed = mixed.astype(BF16)
        mixed_ref[...] = mixed
        x1 = x_ref[...] + jnp.dot(mixed, w_ref[...], preferred_element_type=F32)
        x1_ref[...] = x1
        rstd = lax.rsqrt(jnp.mean(x1 * x1, axis=-1, keepdims=True) + EPS)
        h2_ref[...] = (x1 * rstd * nw_ref[...]).astype(BF16)

    row = pl.BlockSpec((tm, d), lambda i: (i, 0))
    src_row = pl.BlockSpec((tm, k), lambda i: (i, 0))
    blocks = pl.BlockSpec(w_o.shape, lambda i: (0, 0, 0))
    logit_blk = lambda j: pl.BlockSpec((tm, half), functools.partial(lambda i, j: (i, c0 + j), j=j))
    return _pallas(
        body, name=name, grid=(s // tm,),
        in_specs=[src_row, src_row, blocks, blocks, logit_blk(0), logit_blk(1), logit_blk(2), logit_blk(3),
                  pl.BlockSpec((2, 1, d), lambda i: (0, 0, 0)), pl.BlockSpec((d, d), lambda i: (0, 0)), row,
                  pl.BlockSpec((1, d), lambda i: (0, 0))],
        out_specs=[row] * 5,
        out_shape=[jax.ShapeDtypeStruct((s, d), F32), jax.ShapeDtypeStruct((s, d), F32),
                   jax.ShapeDtypeStruct((s, d), BF16), jax.ShapeDtypeStruct((s, d), F32),
                   jax.ShapeDtypeStruct((s, d), BF16)],
        operands=[attn_b, u3, w_o, w_pw, proj, proj, proj, proj, bg, w_out, x, norm_w], rider=rider)


def _out_proj_bwd_gates(dx1, w_out, proj, bg, y_a, y_b, w_o, w_pw, col0, name, after=()):
    s, d = y_a.shape
    n_blk, k, blk = w_o.shape
    tm = 256
    half = d // 2
    assert col0 % half == 0
    c0 = col0 // half
    nt_dims = (((1,), (1,)), ((), ()))

    def body(dx_ref, w_ref, a0_ref, a1_ref, b0_ref, b1_ref, bias_ref, ya_ref, yb_ref, wo_ref, wp_ref,
             dgl_ref, dya_ref, dyb_ref, db_ref, da_ref, du_ref):
        dm = lax.dot_general(dx_ref[...], w_ref[...], nt_dims, preferred_element_type=F32)
        parts = []
        for br, (lo_ref, hi_ref, y_ref, dy_ref, wb_ref, dsrc_ref) in enumerate((
                (a0_ref, a1_ref, ya_ref, dya_ref, wo_ref, da_ref), (b0_ref, b1_ref, yb_ref, dyb_ref, wp_ref, du_ref))):
            logits = jnp.concatenate([lo_ref[...], hi_ref[...]], axis=1)
            gate = jax.nn.sigmoid(logits + bias_ref[br])
            dy = (dm * gate).astype(BF16)
            dy_ref[...] = dy
            dsrc = lax.dot_general(dy[:, 0:blk], wb_ref[0], nt_dims, preferred_element_type=F32)
            for j in range(1, n_blk):
                dsrc = dsrc + lax.dot_general(dy[:, j * blk:(j + 1) * blk], wb_ref[j], nt_dims,
                                              preferred_element_type=F32)
            dsrc_ref[...] = dsrc
            dgl = dm * y_ref[...] * gate * (1.0 - gate)
            dgl_ref[:, br * d:(br + 1) * d] = dgl.astype(BF16)
            parts.append(jnp.sum(dgl, axis=0, keepdims=True))
        part = jnp.concatenate(parts, axis=0)
        first = pl.program_id(0) == 0

        @pl.when(first)
        def _():
            db_ref[...] = part

        @pl.when(jnp.logical_not(first))
        def _():
            db_ref[...] += part

    row = pl.BlockSpec((tm, d), lambda i: (i, 0))
    logit_blk = lambda k: pl.BlockSpec((tm, half), functools.partial(lambda i, k: (i, c0 + k), k=k))
    blocks = pl.BlockSpec(w_o.shape, lambda i: (0, 0, 0))
    src_row = pl.BlockSpec((tm, k), lambda i: (i, 0))
    return _pallas(
        body, name=name, grid=(s // tm,),
        in_specs=[row, pl.BlockSpec((d, d), lambda i: (0, 0)), logit_blk(0), logit_blk(1), logit_blk(2),
                  logit_blk(3), pl.BlockSpec((2, 1, d), lambda i: (0, 0, 0)), row, row, blocks, blocks],
        out_specs=[pl.BlockSpec((tm, 2 * d), lambda i: (i, 0)), row, row, pl.BlockSpec((2, d), lambda i: (0, 0)),
                   src_row, src_row],
        out_shape=[jax.ShapeDtypeStruct((s, 2 * d), BF16), jax.ShapeDtypeStruct((s, d), BF16),
                   jax.ShapeDtypeStruct((s, d), BF16), jax.ShapeDtypeStruct((2, d), F32),
                   jax.ShapeDtypeStruct((s, k), F32), jax.ShapeDtypeStruct((s, k), F32)],
        operands=[dx1, w_out, proj, proj, proj, proj, bg, y_a, y_b, w_o, w_pw], after=after)


def _ffn_in_swiglu(h2, w_blocked, name):
    s, k = h2.shape
    nblk, _, tn = w_blocked.shape
    ff = nblk // 2 * tn
    tm = 512

    def body(a_ref, wg_ref, wu_ref, g_ref, u_ref, act_ref):
        a = a_ref[...]
        gt = jnp.dot(a, wg_ref[...], preferred_element_type=F32)
        up = jnp.dot(a, wu_ref[...], preferred_element_type=F32)
        g_ref[...] = gt
        u_ref[...] = up
        act_ref[...] = (gt * jax.nn.sigmoid(gt) * up).astype(BF16)

    out = pl.BlockSpec((tm, tn), lambda j, i: (i, j))
    return pl.pallas_call(
        body, name=name, grid=(nblk // 2, s // tm),
        in_specs=[pl.BlockSpec((tm, k), lambda j, i: (i, 0)),
                  pl.BlockSpec((None, k, tn), lambda j, i: (j, 0, 0)),
                  pl.BlockSpec((None, k, tn), lambda j, i: (nblk // 2 + j, 0, 0))],
        out_specs=[out, out, out],
        out_shape=[jax.ShapeDtypeStruct((s, ff), F32), jax.ShapeDtypeStruct((s, ff), F32),
                   jax.ShapeDtypeStruct((s, ff), BF16)],
        compiler_params=_params(),
    )(h2, w_blocked, w_blocked)


def _ffn_out_bwd_swiglu(dy, w_ffn_out, gate, up, name, rider=None):
    s, d = dy.shape
    ff = gate.shape[1]
    tm = 256
    nt_dims = (((1,), (1,)), ((), ()))

    def body(dy_ref, w_ref, g_ref, u_ref, o_ref):
        dv = lax.dot_general(dy_ref[...], w_ref[...], nt_dims, preferred_element_type=F32)
        gt = g_ref[...]
        sg = jax.nn.sigmoid(gt)
        o_ref[:, 0:ff] = (dv * u_ref[...] * (sg * (1.0 + gt * (1.0 - sg)))).astype(BF16)
        o_ref[:, ff:2 * ff] = (dv * gt * sg).astype(BF16)

    row = pl.BlockSpec((tm, ff), lambda i: (i, 0))
    return _pallas(
        body, name=name, grid=(s // tm,),
        in_specs=[pl.BlockSpec((tm, d), lambda i: (i, 0)), pl.BlockSpec((ff, d), lambda i: (0, 0)), row, row],
        out_specs=pl.BlockSpec((tm, 2 * ff), lambda i: (i, 0)),
        out_shape=jax.ShapeDtypeStruct((s, 2 * ff), BF16), operands=[dy, w_ffn_out, gate, up], rider=rider)


def _ffn_out_loss(act, w_ffn_out, x1, target, name):
    s, k = act.shape
    d = w_ffn_out.shape[1]
    tm = 512

    def body(a_ref, w_ref, x1_ref, t_ref, dy_ref, dyb_ref, loss_ref, acc):
        y = x1_ref[...] + jnp.dot(a_ref[...], w_ref[...], preferred_element_type=F32)
        diff = y - t_ref[...]
        dy = diff * (1.0 / d)
        dy_ref[...] = dy
        dyb_ref[...] = dy.astype(BF16)
        part = jnp.sum((diff * diff).reshape(tm // 8, 8, d), axis=0)
        i = pl.program_id(0)

        @pl.when(i == 0)
        def _():
            acc[...] = part

        @pl.when(i > 0)
        def _():
            acc[...] += part

        @pl.when(i == pl.num_programs(0) - 1)
        def _():
            loss_ref[...] = (0.5 / d) * jnp.sum(jnp.sum(acc[...], axis=1, keepdims=True), axis=0, keepdims=True)

    row = pl.BlockSpec((tm, d), lambda i: (i, 0))
    return pl.pallas_call(
        body, name=name, grid=(s // tm,),
        in_specs=[pl.BlockSpec((tm, k), lambda i: (i, 0)), pl.BlockSpec((k, d), lambda i: (0, 0)), row, row],
        out_specs=[row, row, pl.BlockSpec((1, 1), lambda i: (0, 0))],
        out_shape=[jax.ShapeDtypeStruct((s, d), F32), jax.ShapeDtypeStruct((s, d), BF16),
                   jax.ShapeDtypeStruct((1, 1), F32)],
        scratch_shapes=[pltpu.VMEM((8, d), F32)], compiler_params=_params(),
    )(act, w_ffn_out, x1, target)


LATE_GATHER = ("w_o_attn", "w_pw_conv", "w_out", "w_ffn_in", "w_ffn_out")
EARLY_REDUCE = LATE_GATHER


def _blocks_by_half(g):
    if g.ndim == 2:
        g = g.reshape(N_CHIPS, g.shape[0] // N_CHIPS, g.shape[1])
    return g.reshape(N_CHIPS, 2, g.shape[1] // 2, g.shape[2])


def _forward_backward(x, pos_col, target, wts, first_gather, late_bufs, pos_arr):
    wts = dict(wts)
    consts = _rope_consts()
    bd = consts[3]
    qw2 = jnp.tile(wts["q_norm_w"], (1, LANES // HEAD_DIM))
    kw2 = jnp.tile(wts["k_norm_w"], (1, LANES // HEAD_DIM))
    qkv_w = 3 * N_SLOT_HEADS * HEAD_DIM
    conv_col0 = 3 * qkv_w

    h = _rmsnorm_fwd(x, wts["norm1_w"], "rms1_fwd")
    slot_order = jnp.bitwise_xor(pos_arr[1], jnp.asarray([0, 2, 1, 3], jnp.int32))
    blocked = lambda buf: buf.reshape(N_CHIPS, -1, buf.shape[3])
    near = first_gather
    proj = _proj_by_slot(h, blocked(near[2][0]), slot_order, 0, 1, "mm_proj_own", after=list(late_bufs))
    near = _split_middle(near, after=[proj], name="allgather_w_in_near_forward")
    far, _ = _split_start(_gather_both_legs_rider(near[2][:1], near[2][1:], FAR_CHIPS), "allgather_w_in_far_start")
    _, bufs = _split_wait((near[0], near[1], far[2], near[3]), after=[], name="allgather_w_in_near_wait")
    proj = _proj_by_slot(h, blocked(bufs[0]), slot_order, 1, len(NEAR_CHIPS), "mm_proj_near", proj=proj)
    far = _split_middle((far[0], far[1], bufs, far[3]), after=[proj], name="allgather_w_in_far_forward")
    (w_in_buf, conv_w_buf, b_gate_buf), _ = _split_wait(far, after=[], name="allgather_w_in_far_wait")
    wts["w_in"] = w_in_buf.reshape(N_CHIPS, -1, w_in_buf.shape[3])
    wts["conv_w"] = conv_w_buf.transpose(1, 0, 2).reshape(CONV_WIDTH, -1)
    wts["b_gate"] = b_gate_buf.transpose(1, 0, 2).reshape(2, 1, -1)
    ch = wts["conv_w"].shape[1]
    gate_col0 = conv_col0 + 2 * ch
    n_mix = LATE_GATHER.index("w_ffn_in")
    mix_rider, ffn_rider = _gather_ici_rider(late_bufs[:n_mix], []), _gather_ici_rider(late_bufs[n_mix:], [])
    both, started = _split_start(_riders_together(mix_rider, ffn_rider), "late_gather_start", after=[wts["w_in"]])
    n_sem, n_buf = len(mix_rider.scratch), len(mix_rider.operands)
    mix_gather = (mix_rider, both[1][:n_sem], both[2][:n_buf], [])
    ffn_gather = (ffn_rider, both[1][n_sem:], both[2][n_buf:], [])
    proj = _proj_by_slot(h, wts["w_in"], slot_order, 1 + len(NEAR_CHIPS), len(FAR_CHIPS), "mm_proj_far", proj=proj,
                         after=[started])
    qn, kn = _qk_fwd(proj, pos_col, qw2, kw2, consts, "qk_fwd")
    attn, lse, attn_b = _attn_fwd(qn, kn, proj, "attn_fwd")

    def gathered(names, bufs):
        for n, buf in zip(names, bufs):
            full = buf.reshape(N_CHIPS, -1, buf.shape[3])
            wts[n] = full.reshape(-1, full.shape[2]) if n in ROW_SHARDED else full

    mix_bufs, _ = _split_wait(mix_gather, after=[attn_b], name="late_gather_mix_wait")
    (u0, uc), mix_bufs = _conv_fwd(proj, wts["conv_w"], wts["conv_b"], conv_col0, "conv_fwd",
                                   rider=_gather_forward_rider(mix_bufs))
    gathered(LATE_GATHER[:n_mix], mix_bufs)
    u3 = _ln_silu_fwd(uc, wts["conv_ln_w"], wts["conv_ln_b"], "ln_fwd")
    ffn_bufs, _ = _split_wait(ffn_gather, after=[u3], name="late_gather_ffn_wait")
    (y_a, y_b, mixed, x1, h2), ffn_bufs = _mix_out_proj(
        attn_b, u3, wts["w_o_attn"], wts["w_pw_conv"], proj, wts["b_gate"], gate_col0, wts["w_out"], x,
        wts["norm2_w"], "mix_x1_rms2", rider=_gather_forward_rider(ffn_bufs))
    gathered(LATE_GATHER[n_mix:], ffn_bufs)
    gate, up, act = _ffn_in_swiglu(h2, wts["w_ffn_in"], "mm_gu_swiglu")
    dy, dy_b16, loss = _ffn_out_loss(act, wts["w_ffn_out"], x1, target, "mm_x2_loss")

    g = {}
    by_chip = {}

    def pair_add(n, blocks, received):
        return _add_own_half(blocks, received, pos_arr, f"grads_pair_add_{n}")

    g_ffn_out = _blocks_by_half(
        _matmul(act, dy_b16, mode="tn", tm=1408, tn=1024, tk=2048, out_dtype=F32, name="mm_dwffnout"))
    dgu, (received,) = _ffn_out_bwd_swiglu(dy_b16, wts["w_ffn_out"], gate, up, "mm_dact_swiglu_bwd",
                                           rider=_pair_exchange_rider([g_ffn_out], halved=True))
    to_send, own = pair_add("w_ffn_out", g_ffn_out, received)
    (dx1, dx1_b16, g["norm2_w"]), (by_chip["w_ffn_out"],) = _matmul_nt_rmsnorm_bwd(
        dgu, wts["w_ffn_in"], x1, wts["norm2_w"], dy, "mm_dh2_rms2_bwd", rider=_chip_exchange_rider([to_send], [own]))
    g_ffn_in = _blocks_by_half(_matmul(h2, dgu, mode="tn", tm=512, tn=1408, tk=2048, out_dtype=F32,
                                       name="mm_dwffnin", out_blocked=N_CHIPS, cols_outer=True))
    exchanging, started = _split_start(_pair_exchange_rider([g_ffn_in], halved=True), "grads_ffn_in_pair_start")
    g["w_out"] = _matmul(mixed, dx1_b16, mode="tn", tm=512, tn=1024, tk=2048, out_dtype=F32, name="mm_dwout",
                         after=[started])
    dgl, dy_a, dy_b, g["b_gate"], dattn, du3 = _out_proj_bwd_gates(
        dx1_b16, wts["w_out"], proj, wts["b_gate"], y_a, y_b, wts["w_o_attn"], wts["w_pw_conv"], gate_col0,
        "mix_bwd")
    (received,), (g_ffn_in,) = _split_wait(exchanging, after=[dgl], name="grads_ffn_in_pair_wait")
    ffn_in_to_send, ffn_in_own = pair_add("w_ffn_in", g_ffn_in, received)
    g["w_o_attn"] = _matmul(attn_b, dy_a, mode="tn", tm=512, tn=256, tk=2048, out_dtype=F32, name="mm_dwo",
                            out_blocked=N_CHIPS)
    g["w_pw_conv"] = _matmul(u3, dy_b, mode="tn", tm=512, tn=256, tk=2048, out_dtype=F32, name="mm_dwpw",
                             out_blocked=N_CHIPS)
    duc, g["conv_ln_w"], g["conv_ln_b"] = _ln_silu_bwd(du3, uc, wts["conv_ln_w"], wts["conv_ln_b"], "ln_bwd")

    small3 = ("w_out", "w_o_attn", "w_pw_conv")
    g_small3 = [_blocks_by_half(g.pop(n)) for n in small3]
    (da, db, g["conv_w"], g["conv_b"]), received = _conv_bwd(
        duc, u0, proj, wts["conv_w"], conv_col0, "conv_bwd", rider=_pair_exchange_rider(g_small3, halved=True))
    sums3 = [pair_add(n, gb, rv) for n, gb, rv in zip(small3, g_small3, received)]
    (dqn, dkn, dv), (by_chip["w_ffn_in"],) = _attn_bwd(
        qn, kn, proj, dattn, attn, lse, bd, "attn_bwd",
        rider=_chip_exchange_rider([ffn_in_to_send], [ffn_in_own]))
    (dproj, dqw, dkw), exchanged3 = _qk_bwd(
        dqn, dkn, dv, da, db, dgl, proj, pos_col, qw2, kw2, consts, "qk_bwd",
        rider=_chip_exchange_rider([s[0] for s in sums3], [s[1] for s in sums3]))
    by_chip.update(zip(small3, exchanged3))
    halves = [_sum_chips(by_chip[n], pos_arr, f"grads_chip_sum_{n}") for n in EARLY_REDUCE]
    g["q_norm_w"] = dqw[:, :HEAD_DIM]
    g["k_norm_w"] = dkw[:, :HEAD_DIM]

    c = pos_arr[0]
    rh = h.shape[1] // 2
    h_sibling = lax.dynamic_slice_in_dim(h, (1 - c) * rh, rh, axis=1)
    h_own = lax.dynamic_slice_in_dim(h, c * rh, rh, axis=1)
    g_sibling, shards = _matmul(h_sibling, dproj, mode="tn", tm=rh, tn=1920, tk=2048, out_dtype=F32,
                                name="mm_dwin_sibling", out_blocked=N_CHIPS, rider=_pair_gather_rider(halves))
    reduced = dict(zip(EARLY_REDUCE, shards))
    exchanging, started = _split_start(_pair_exchange_rider([g_sibling], halved=False), "grads_w_in_pair_start")
    g_own = _matmul(h_own, dproj, mode="tn", tm=rh, tn=1920, tk=2048, out_dtype=F32, name="mm_dwin_own",
                    out_blocked=N_CHIPS, after=[started])
    (from_sibling,), _ = _split_wait(exchanging, after=[g_own], name="grads_w_in_pair_wait")
    to_send, own = _add_own_half(g_own, from_sibling, pos_arr, "grads_pair_add_w_in")
    in_flight, started = _split_start(_chip_exchange_rider([to_send], [own]), "grads_w_in_exchange_start")
    grad_x, _, g["norm1_w"] = _matmul_nt_rmsnorm_bwd(dproj, wts["w_in"], x, wts["norm1_w"], dx1, "mm_dh_rms1_bwd",
                                                     after=[started])
    return loss, grad_x, g, reduced, (in_flight, started)


def _mesh_pos():
    return lax.axis_index("x"), lax.axis_index("y"), lax.axis_index("c")


def _other_chips(x, y):
    return [(1 - x, y), (x, 1 - y), (1 - x, 1 - y)]


NEAR_CHIPS, FAR_CHIPS = (0, 1), (2,)


def _cast_into_slot(shard, chip_arr, dtype, name, n_slots=N_CHIPS, after=()):
    r, c = shard.shape
    tr = r // 2 if r % 32 == 0 else r

    def body(chip_ref, s_ref, *refs):
        refs[-1][...] = s_ref[...].astype(dtype)

    return pl.pallas_call(
        body, name=name,
        grid_spec=pltpu.PrefetchScalarGridSpec(
            num_scalar_prefetch=1, grid=(r // tr,),
            in_specs=[pl.BlockSpec((tr, c), lambda i, chip_ref: (i, 0))] + [ANY] * len(after),
            out_specs=pl.BlockSpec((None, tr, c), lambda i, chip_ref: (chip_ref[0], i, 0))),
        out_shape=jax.ShapeDtypeStruct((n_slots, r, c), dtype), compiler_params=_params(),
    )(chip_arr, shard, *after)


GATHER_CHUNKS = 4


def _gather_both_legs_rider(big, small, peers):
    nb = len(big)
    n = nb + len(small)
    nch = GATHER_CHUNKS

    def part(bufs, a, slot, half, ch):
        if a >= nb:
            return bufs[a].at[slot]
        rows = bufs[a].shape[2] // nch
        return bufs[a].at[slot, half, pl.ds(ch * rows, rows)]

    def pieces():
        return [(a, ch, k) for ch in range(nch) for a in range(n) for k in peers if a < nb or ch == 0]

    def ici(bufs, sems, a, ch, k, slot_of_src):
        x, y, c = _mesh_pos()
        px, py = _other_chips(x, y)[k]
        slot = 2 * x + y if slot_of_src == "mine" else 2 * px + py
        return pltpu.make_async_remote_copy(
            src_ref=part(bufs, a, slot, c, ch), dst_ref=part(bufs, a, slot, c, ch), send_sem=sems[0].at[a, ch, k],
            recv_sem=sems[1].at[a, ch, k], device_id=(px, py, c), device_id_type=MESH)

    def forward(bufs, sems, a, ch, k, half):
        x, y, c = _mesh_pos()
        px, py = _other_chips(x, y)[k]
        h = c if half == "mine" else 1 - c
        return pltpu.make_async_remote_copy(
            src_ref=part(bufs, a, 2 * px + py, h, ch), dst_ref=part(bufs, a, 2 * px + py, h, ch),
            send_sem=sems[2].at[a, ch, k], recv_sem=sems[3].at[a, ch, k], device_id=(x, y, 1 - c),
            device_id_type=MESH)

    def start(r_in, bufs, sems):
        for a, ch, k in pieces():
            ici(bufs, sems, a, ch, k, "mine").start()

    def middle(r_in, bufs, sems):
        for a, ch, k in pieces():
            ici(bufs, sems, a, ch, k, "theirs").wait_recv()
            if a < nb:
                forward(bufs, sems, a, ch, k, "mine").start()
        for a, ch, k in pieces():
            ici(bufs, sems, a, ch, k, "mine").wait_send()

    def wait(r_in, bufs, sems):
        for a, ch, k in pieces():
            if a < nb:
                forward(bufs, sems, a, ch, k, "theirs").wait_recv()
        for a, ch, k in pieces():
            if a < nb:
                forward(bufs, sems, a, ch, k, "mine").wait_send()

    ops = list(big) + list(small)
    return _Rider(ops, [jax.ShapeDtypeStruct(o.shape, o.dtype) for o in ops], {i: i for i in range(n)},
                  [pltpu.SemaphoreType.DMA((n, nch, 3)), pltpu.SemaphoreType.DMA((n, nch, 3)),
                   pltpu.SemaphoreType.DMA((nb, nch, 3)), pltpu.SemaphoreType.DMA((nb, nch, 3))],
                  start, wait, middle)


def _comm_call(rider, name):
    def body():
        pass

    return _pallas(body, name=name, grid=(1,), in_specs=[], out_specs=[], out_shape=[], operands=[],
                   rider=rider)[1]


def _gather_ici_rider(big, small):
    nb = len(big)
    n = nb + len(small)

    def copies(bufs, sems):
        x, y, c = _mesh_pos()
        me = 2 * x + y
        part = lambda a, slot: bufs[a].at[slot, c] if a < nb else bufs[a].at[slot]
        out = []
        for a in range(n):
            for k, (px, py) in enumerate(_other_chips(x, y)):
                send = functools.partial(
                    pltpu.make_async_remote_copy,
                    src_ref=part(a, me), dst_ref=part(a, me), send_sem=sems[0].at[a, k],
                    recv_sem=sems[1].at[a, k], device_id=(px, py, c), device_id_type=MESH)
                recv = functools.partial(
                    pltpu.make_async_remote_copy,
                    src_ref=part(a, 2 * px + py), dst_ref=part(a, 2 * px + py), send_sem=sems[0].at[a, k],
                    recv_sem=sems[1].at[a, k], device_id=(px, py, c), device_id_type=MESH)
                out.append((send, recv))
        return out

    def start(r_in, r_out, sems):
        for send, _ in copies(r_out, sems):
            send().start()

    def wait(r_in, r_out, sems):
        cps = copies(r_out, sems)
        for _, recv in cps:
            recv().wait_recv()
        for send, _ in cps:
            send().wait_send()

    ops = list(big) + list(small)
    return _Rider(ops, [jax.ShapeDtypeStruct(o.shape, o.dtype) for o in ops], {i: i for i in range(n)},
                  [pltpu.SemaphoreType.DMA((n, 3)), pltpu.SemaphoreType.DMA((n, 3))], start, wait)


def _gather_forward_rider(big):
    n = len(big)

    def copies(bufs, sems):
        x, y, c = _mesh_pos()
        out = []
        for a in range(n):
            for k, (px, py) in enumerate(_other_chips(x, y)):
                slot = 2 * px + py
                send = functools.partial(
                    pltpu.make_async_remote_copy,
                    src_ref=bufs[a].at[slot, c], dst_ref=bufs[a].at[slot, c], send_sem=sems[0].at[a, k],
                    recv_sem=sems[1].at[a, k], device_id=(x, y, 1 - c), device_id_type=MESH)
                recv = functools.partial(
                    pltpu.make_async_remote_copy,
                    src_ref=bufs[a].at[slot, 1 - c], dst_ref=bufs[a].at[slot, 1 - c], send_sem=sems[0].at[a, k],
                    recv_sem=sems[1].at[a, k], device_id=(x, y, 1 - c), device_id_type=MESH)
                out.append((send, recv))
        return out

    def start(r_in, r_out, sems):
        for send, _ in copies(r_out, sems):
            send().start()

    def wait(r_in, r_out, sems):
        cps = copies(r_out, sems)
        for _, recv in cps:
            recv().wait_recv()
        for send, _ in cps:
            send().wait_send()

    return _Rider(big, [jax.ShapeDtypeStruct(o.shape, o.dtype) for o in big], {i: i for i in range(n)},
                  [pltpu.SemaphoreType.DMA((n, 3)), pltpu.SemaphoreType.DMA((n, 3))], start, wait)


def _pair_exchange_rider(gs, halved):
    n = len(gs)

    def copies(r_in, r_out, sems):
        x, y, c = _mesh_pos()
        return [pltpu.make_async_remote_copy(
            src_ref=r_in[a].at[:, 1 - c] if halved else r_in[a], dst_ref=r_out[a], send_sem=sems[0].at[a],
            recv_sem=sems[1].at[a], device_id=(x, y, 1 - c), device_id_type=MESH) for a in range(n)]

    def start(r_in, r_out, sems):
        for cp in copies(r_in, r_out, sems):
            cp.start()

    def wait(r_in, r_out, sems):
        for cp in copies(r_in, r_out, sems):
            cp.wait()

    return _Rider(gs, [jax.ShapeDtypeStruct((g.shape[0],) + g.shape[-2:], g.dtype) for g in gs], {},
                  [pltpu.SemaphoreType.DMA((n,)), pltpu.SemaphoreType.DMA((n,))], start, wait)


def _chip_exchange_rider(to_send, by_chip, row_range=None):
    n = len(to_send)

    def copies(r_in, r_out, sems):
        x, y, c = _mesh_pos()
        me = 2 * x + y
        rows = (lambda ref: ref) if row_range is None else (lambda ref: ref.at[pl.ds(*row_range)])
        out = []
        for a in range(n):
            for k, (px, py) in enumerate(_other_chips(x, y)):
                send = functools.partial(
                    pltpu.make_async_remote_copy,
                    src_ref=rows(r_in[a].at[2 * px + py]), dst_ref=rows(r_out[a].at[me]),
                    send_sem=sems[0].at[a, k], recv_sem=sems[1].at[a, k], device_id=(px, py, c),
                    device_id_type=MESH)
                recv = functools.partial(
                    pltpu.make_async_remote_copy,
                    src_ref=rows(r_in[a].at[me]), dst_ref=rows(r_out[a].at[2 * px + py]),
                    send_sem=sems[0].at[a, k], recv_sem=sems[1].at[a, k], device_id=(px, py, c),
                    device_id_type=MESH)
                out.append((send, recv))
        return out

    def start(r_in, r_out, sems):
        for send, _ in copies(r_in, r_out, sems):
            send().start()

    def wait(r_in, r_out, sems):
        cps = copies(r_in, r_out, sems)
        for _, recv in cps:
            recv().wait_recv()
        for send, _ in cps:
            send().wait_send()

    return _Rider(list(to_send) + list(by_chip), [jax.ShapeDtypeStruct(b.shape, b.dtype) for b in by_chip],
                  {n + i: i for i in range(n)},
                  [pltpu.SemaphoreType.DMA((n, 3)), pltpu.SemaphoreType.DMA((n, 3))], start, wait)


HBM = pl.BlockSpec(memory_space=pltpu.HBM)
SEM = pl.BlockSpec(memory_space=pltpu.SEMAPHORE)


_IN_FLIGHT = pltpu.CompilerParams(has_side_effects=pltpu.SideEffectType.DATAFLOW_SIDE_EFFECTING)


class _FlatSems:
    def __init__(self, ref, shape):
        self.ref, self.shape = ref, shape

    @property
    def at(self):
        return self

    def __getitem__(self, idx):
        idx = idx if isinstance(idx, tuple) else (idx,)
        flat = 0
        for i, n in zip(idx, self.shape):
            flat = flat * n + i
        return self.ref.at[flat]


def _flat_sem_types(rider):
    return tuple(pltpu.SemaphoreType.DMA((int(np.prod(s.shape)),)) for s in rider.scratch)


def _as_rider_sems(rider, refs):
    return [_FlatSems(r, s.shape) for r, s in zip(refs, rider.scratch)]


def _split_start(rider, name, after=()):
    n_in, n_out, n_sem = len(rider.operands), len(rider.out_shapes), len(rider.scratch)
    n_after = len(after)
    fresh = [j for j in range(n_out) if j not in rider.aliases.values()]
    by_out = {j: i for i, j in rider.aliases.items()}

    def body(*refs):
        r_in = refs[:n_in]
        refs = refs[n_in + n_after:]
        sems = refs[:n_sem]
        thru = refs[n_sem:n_sem + n_in]
        fresh_refs = refs[n_sem + n_in:n_sem + n_in + len(fresh)]
        token = refs[-1]
        r_out = [thru[by_out[j]] if j in by_out else fresh_refs[fresh.index(j)] for j in range(n_out)]
        rider.start(r_in, r_out, _as_rider_sems(rider, sems))
        token[...] = jnp.zeros_like(token)

    res = pl.pallas_call(
        body, name=name,
        out_shape=_flat_sem_types(rider) + tuple(pltpu.HBM(o.shape, o.dtype) for o in rider.operands)
        + tuple(pltpu.HBM(rider.out_shapes[j].shape, rider.out_shapes[j].dtype) for j in fresh)
        + (jax.ShapeDtypeStruct((8, LANES), F32),),
        in_specs=(HBM,) * n_in + (ANY,) * n_after,
        out_specs=(SEM,) * n_sem + (HBM,) * (n_in + len(fresh)) + (pl.BlockSpec(memory_space=pltpu.VMEM),),
        input_output_aliases={i: n_sem + i for i in range(n_in)}, compiler_params=_IN_FLIGHT,
    )(*[pltpu.with_memory_space_constraint(o, pltpu.HBM) for o in rider.operands], *after)
    return (rider, res[:n_sem], res[n_sem:n_sem + n_in], res[n_sem + n_in:-1]), res[-1]


def _split_continue(handles, after, name, phase):
    rider, sems, thru, fresh_arrays = handles
    n_in, n_out, n_sem = len(rider.operands), len(rider.out_shapes), len(rider.scratch)
    fresh = [j for j in range(n_out) if j not in rider.aliases.values()]
    by_out = {j: i for i, j in rider.aliases.items()}
    n_data = n_in + len(fresh)

    def body(*refs):
        r_in = refs[:n_in]
        fresh_refs = refs[n_in:n_data]
        sem_refs = refs[n_data:n_data + n_sem]
        r_out = [r_in[by_out[j]] if j in by_out else fresh_refs[fresh.index(j)] for j in range(n_out)]
        phase(r_in, r_out, _as_rider_sems(rider, sem_refs))

    data = list(thru) + list(fresh_arrays)
    return pl.pallas_call(
        body, name=name, out_shape=tuple(pltpu.HBM(d.shape, d.dtype) for d in data),
        in_specs=(HBM,) * n_data + (SEM,) * n_sem + (ANY,) * len(after), out_specs=(HBM,) * n_data,
        input_output_aliases={i: i for i in range(n_data)}, compiler_params=_IN_FLIGHT,
    )(*data, *sems, *after)


def _split_middle(handles, after, name):
    rider, sems, thru, _ = handles
    res = _split_continue(handles, after, name, rider.middle)
    return rider, sems, res[:len(thru)], res[len(thru):]


def _split_wait(handles, after, name):
    rider = handles[0]
    n_in, n_out = len(rider.operands), len(rider.out_shapes)
    fresh = [j for j in range(n_out) if j not in rider.aliases.values()]
    by_out = {j: i for i, j in rider.aliases.items()}
    res = _split_continue(handles, after, name, rider.wait)
    return [res[by_out[j]] if j in by_out else res[n_in + fresh.index(j)] for j in range(n_out)], res[:n_in]


def _pair_gather_rider(bufs):
    n = len(bufs)

    def copies(r_out, sems):
        x, y, c = _mesh_pos()
        out = []
        for a in range(n):
            send = functools.partial(
                    pltpu.make_async_remote_copy,
                src_ref=r_out[a].at[c], dst_ref=r_out[a].at[c], send_sem=sems[0].at[a],
                recv_sem=sems[1].at[a], device_id=(x, y, 1 - c), device_id_type=MESH)
            recv = functools.partial(
                    pltpu.make_async_remote_copy,
                src_ref=r_out[a].at[1 - c], dst_ref=r_out[a].at[1 - c], send_sem=sems[0].at[a],
                recv_sem=sems[1].at[a], device_id=(x, y, 1 - c), device_id_type=MESH)
            out.append((send, recv))
        return out

    def start(r_in, r_out, sems):
        for send, _ in copies(r_out, sems):
            send().start()

    def wait(r_in, r_out, sems):
        cps = copies(r_out, sems)
        for _, recv in cps:
            recv().wait_recv()
        for send, _ in cps:
            send().wait_send()

    return _Rider(bufs, [jax.ShapeDtypeStruct(b.shape, b.dtype) for b in bufs], {i: i for i in range(n)},
                  [pltpu.SemaphoreType.DMA((n,)), pltpu.SemaphoreType.DMA((n,))], start, wait)


def _add_own_half(g, recv, pos_arr, name):
    nb, rh, cols = g.shape[0], g.shape[-2], g.shape[-1]

    def body(pos_ref, g_ref, r_ref, send_ref, own_ref):
        s = (g_ref[...] + r_ref[...]).astype(BF16)
        send_ref[...] = s

        @pl.when(pl.program_id(0) == pos_ref[1])
        def _():
            own_ref[...] = s

    blk = pl.BlockSpec((None, rh, cols), lambda j, pos_ref: (j, 0, 0))
    g_spec = blk if g.ndim == 3 else pl.BlockSpec((None, None, rh, cols),
                                                   lambda j, pos_ref: (j, pos_ref[0], 0, 0))
    shape = jax.ShapeDtypeStruct((nb, rh, cols), BF16)
    return pl.pallas_call(
        body, name=name,
        grid_spec=pltpu.PrefetchScalarGridSpec(
            num_scalar_prefetch=1, grid=(nb,), in_specs=[g_spec, blk],
            out_specs=[blk, pl.BlockSpec((None, rh, cols), lambda j, pos_ref: (pos_ref[1], 0, 0))]),
        out_shape=[shape, shape], compiler_params=_params(),
    )(pos_arr, g, recv)


def _sum_chips(gath, pos_arr, name):
    nb, rh, cols = gath.shape

    def body(pos_ref, a_ref, b_ref, c_ref, d_ref, o_ref):
        del pos_ref
        o_ref[...] = ((a_ref[...].astype(F32) + b_ref[...].astype(F32)) + c_ref[...].astype(F32)) \
            + d_ref[...].astype(F32)

    tr = rh // 2 if (rh // 2) % 16 == 0 else rh
    specs = [pl.BlockSpec((None, tr, cols), functools.partial(lambda i, pos_ref, j: (j, i, 0), j=j))
             for j in range(nb)]
    return pl.pallas_call(
        body, name=name,
        grid_spec=pltpu.PrefetchScalarGridSpec(
            num_scalar_prefetch=1, grid=(rh // tr,), in_specs=specs,
            out_specs=pl.BlockSpec((None, tr, cols), lambda i, pos_ref: (pos_ref[0], i, 0))),
        out_shape=jax.ShapeDtypeStruct((2, rh, cols), F32), compiler_params=_params(),
    )(pos_arr, gath, gath, gath, gath)


N_DEVICES = 8


def _small_gather_rider(buf):
    def copies(r_out, sems):
        x, y, c = _mesh_pos()
        me = 4 * x + 2 * y + c
        out = []
        for r in range(1, N_DEVICES):
            px = 1 - x if r & 4 else x
            py = 1 - y if r & 2 else y
            pc = 1 - c if r & 1 else c
            out.append(pltpu.make_async_remote_copy(
                src_ref=r_out[0].at[me], dst_ref=r_out[0].at[me], send_sem=sems[0].at[r - 1],
                recv_sem=sems[1].at[r - 1], device_id=(px, py, pc), device_id_type=MESH))
        return out

    def start(r_in, r_out, sems):
        for cp in copies(r_out, sems):
            cp.start()

    def wait(r_in, r_out, sems):
        cps = copies(r_out, sems)
        for cp in cps:
            cp.wait_recv()
        for cp in cps:
            cp.wait_send()

    return _Rider([buf], [jax.ShapeDtypeStruct(buf.shape, buf.dtype)], {0: 0},
                  [pltpu.SemaphoreType.DMA((N_DEVICES - 1,)), pltpu.SemaphoreType.DMA((N_DEVICES - 1,))],
                  start, wait)


def _sum_devices(buf, name):
    def body(b_ref, o_ref):
        acc = b_ref[0]
        for i in range(1, N_DEVICES):
            acc = acc + b_ref[i]
        o_ref[...] = acc

    return _pallas(body, name=name, grid=(1,), in_specs=[pl.BlockSpec(buf.shape, lambda i: (0, 0, 0))],
                   out_specs=pl.BlockSpec(buf.shape[1:], lambda i: (0, 0)),
                   out_shape=jax.ShapeDtypeStruct(buf.shape[1:], F32), operands=[buf])


def _adamw_math(w, g, m, v):
    m = ADAM_B1 * m + (1.0 - ADAM_B1) * g
    v = ADAM_B2 * v + (1.0 - ADAM_B2) * (g * g)
    m_hat = m / (1.0 - ADAM_B1 ** ADAM_STEP)
    v_hat = v / (1.0 - ADAM_B2 ** ADAM_STEP)
    delta = -ADAM_LR * (m_hat / (jnp.sqrt(v_hat) + ADAM_EPS) + ADAM_WD * w)
    return delta, m, v


def _adamw(w, g, m, v, name, after=()):
    r, c = w.shape
    tr = next(t for t in (256, 352, 128, 64) if r % t == 0 and r >= 2 * t)

    def body(w_ref, g_ref, m_ref, v_ref, go_ref, d_ref, mo_ref, vo_ref):
        gv = g_ref[...]
        d, mn, vn = _adamw_math(w_ref[...], gv, m_ref[...], v_ref[...])
        go_ref[...] = gv
        d_ref[...] = d
        mo_ref[...] = mn
        vo_ref[...] = vn

    blk = pl.BlockSpec((tr, c), lambda i: (i, 0))
    return _pallas(body, name=name, grid=(r // tr,), in_specs=[blk] * 4, out_specs=[blk] * 4,
                   out_shape=[jax.ShapeDtypeStruct((r, c), F32)] * 4, operands=[w, g, m, v], after=after)


def _adamw_small(ws, gs, ms, vs, name):
    n = len(ws)

    def body(*refs):
        w_r, g_r, m_r, v_r = refs[:n], refs[n:2 * n], refs[2 * n:3 * n], refs[3 * n:4 * n]
        d_o, m_o, v_o = refs[4 * n:5 * n], refs[5 * n:6 * n], refs[6 * n:7 * n]
        for i in range(n):
            d, mn, vn = _adamw_math(w_r[i][...], g_r[i][...], m_r[i][...], v_r[i][...])
            d_o[i][...] = d
            m_o[i][...] = mn
            v_o[i][...] = vn

    specs = [pl.BlockSpec(w.shape, lambda i: (0, 0)) for w in ws]
    shapes = [jax.ShapeDtypeStruct(w.shape, F32) for w in ws]
    outs = pl.pallas_call(
        body, name=name, grid=(1,), in_specs=specs * 4, out_specs=specs * 3, out_shape=shapes * 3,
        compiler_params=_params(),
    )(*ws, *gs, *ms, *vs)
    return outs[:n], outs[n:2 * n], outs[2 * n:]


BIG = ("w_in", "w_o_attn", "w_pw_conv", "w_out", "w_ffn_in", "w_ffn_out")
ROW_SHARDED = ("w_out", "w_ffn_out")
SMALL = ("norm1_w", "b_gate", "q_norm_w", "k_norm_w", "conv_w", "conv_b", "conv_ln_w", "conv_ln_b", "norm2_w")
ORDER = ("norm1_w", "w_in", "b_gate", "q_norm_w", "k_norm_w", "w_o_attn", "conv_w", "conv_b", "conv_ln_w",
         "conv_ln_b", "w_pw_conv", "w_out", "norm2_w", "w_ffn_in", "w_ffn_out")
PACK_TILE = 8 * LANES


def _pack_small(parts):
    rows = []
    for p in parts:
        flat = p.reshape(-1)
        pad = (-flat.shape[0]) % PACK_TILE
        rows.append(jnp.pad(flat, (0, pad)).reshape(-1, LANES))
    return jnp.concatenate(rows, axis=0)


def _unpack_small(packed, shapes):
    out, row = [], 0
    for shp in shapes:
        size = int(np.prod(shp))
        nrow = -(-size // PACK_TILE) * (PACK_TILE // LANES)
        out.append(packed[row:row + nrow].reshape(-1)[:size].reshape(shp))
        row += nrow
    return out


def kernel(x, positions, norm1_w, w_in, b_gate, q_norm_w, k_norm_w, w_o_attn, conv_w, conv_b, conv_ln_w, conv_ln_b, w_pw_conv, w_out, norm2_w, w_ffn_in, w_ffn_out, loss_target, m_norm1_w, m_w_in, m_b_gate, m_q_norm_w, m_k_norm_w, m_w_o_attn, m_conv_w, m_conv_b, m_conv_ln_w, m_conv_ln_b, m_w_pw_conv, m_w_out, m_norm2_w, m_w_ffn_in, m_w_ffn_out, v_norm1_w, v_w_in, v_b_gate, v_q_norm_w, v_k_norm_w, v_w_o_attn, v_conv_w, v_conv_b, v_conv_ln_w, v_conv_ln_b, v_w_pw_conv, v_w_out, v_norm2_w, v_w_ffn_in, v_w_ffn_out):
    w = dict(norm1_w=norm1_w, w_in=w_in, b_gate=b_gate, q_norm_w=q_norm_w, k_norm_w=k_norm_w, w_o_attn=w_o_attn,
             conv_w=conv_w, conv_b=conv_b, conv_ln_w=conv_ln_w, conv_ln_b=conv_ln_b, w_pw_conv=w_pw_conv,
             w_out=w_out, norm2_w=norm2_w, w_ffn_in=w_ffn_in, w_ffn_out=w_ffn_out)
    m = dict(norm1_w=m_norm1_w, w_in=m_w_in, b_gate=m_b_gate, q_norm_w=m_q_norm_w, k_norm_w=m_k_norm_w,
             w_o_attn=m_w_o_attn, conv_w=m_conv_w, conv_b=m_conv_b, conv_ln_w=m_conv_ln_w,
             conv_ln_b=m_conv_ln_b, w_pw_conv=m_w_pw_conv, w_out=m_w_out, norm2_w=m_norm2_w,
             w_ffn_in=m_w_ffn_in, w_ffn_out=m_w_ffn_out)
    v = dict(norm1_w=v_norm1_w, w_in=v_w_in, b_gate=v_b_gate, q_norm_w=v_q_norm_w, k_norm_w=v_k_norm_w,
             w_o_attn=v_w_o_attn, conv_w=v_conv_w, conv_b=v_conv_b, conv_ln_w=v_conv_ln_w,
             conv_ln_b=v_conv_ln_b, w_pw_conv=v_w_pw_conv, w_out=v_w_out, norm2_w=v_norm2_w,
             w_ffn_in=v_w_ffn_in, w_ffn_out=v_w_ffn_out)
    cx, cy, cc = _mesh_pos()
    chip = 2 * cx + cy

    chip_arr = chip.reshape(1).astype(jnp.int32)
    pos_arr = jnp.stack([cc, chip]).astype(jnp.int32)
    halves = lambda buf: buf.reshape(N_CHIPS, 2, buf.shape[1] // 2, buf.shape[2])
    w_in_buf = halves(_cast_into_slot(w["w_in"][0], chip_arr, BF16, "cast_w_in"))
    small_bufs = [_cast_into_slot(w[n][0], chip_arr, F32, f"slot_{n}") for n in ("conv_w", "b_gate")]
    first_gather, started = _split_start(_gather_both_legs_rider([w_in_buf], small_bufs, NEAR_CHIPS),
                                         "allgather_w_in_near_start")
    late_bufs = [halves(_cast_into_slot(w[n][0], chip_arr, BF16, f"cast_{n}", after=[started]))
                 for n in LATE_GATHER]
    wts = dict(norm1_w=norm1_w, q_norm_w=q_norm_w, k_norm_w=k_norm_w, conv_b=conv_b, conv_ln_w=conv_ln_w,
               conv_ln_b=conv_ln_b, norm2_w=norm2_w)

    loss, grad_x, g, reduced, w_in_in_flight = _forward_backward(
        x[0], positions.reshape(-1, 1), loss_target[0], wts, first_gather, late_bufs, pos_arr)
    grads = {n: b.reshape(-1, b.shape[2]) for n, b in reduced.items()}

    w_in_in_flight, started = w_in_in_flight
    delta, new_m, new_v = {}, {}, {}
    for n in EARLY_REDUCE:
        grads[n], delta[n], new_m[n], new_v[n] = _adamw(w[n][0], grads[n], m[n][0], v[n][0], f"adamw_{n}",
                                                        after=[started])
    small_parts = [loss] + [g[n] for n in SMALL]
    small_shapes = [p.shape for p in small_parts]
    device_arr = (4 * cx + 2 * cy + cc).reshape(1).astype(jnp.int32)
    small_buf = _cast_into_slot(_pack_small(small_parts), device_arr, F32, "slot_small", n_slots=N_DEVICES)

    (by_chip_w_in,), _ = _split_wait(w_in_in_flight, after=[delta[n] for n in EARLY_REDUCE] + [small_buf],
                                     name="grads_w_in_exchange_wait")
    half_w_in = _sum_chips(by_chip_w_in, pos_arr, "grads_chip_sum_w_in")
    shard_w_in, small_buf = _comm_call(
        _riders_together(_pair_gather_rider([half_w_in]), _small_gather_rider(small_buf)),
        "grads_pair_gather_w_in_small_gather")
    summed = _sum_devices(small_buf, "small_sum")
    reduced = _unpack_small(summed, small_shapes)
    loss_total = reduced[0].reshape(())
    for n, r in zip(SMALL, reduced[1:]):
        grads[n] = r
    ch_shard = conv_w.shape[2]
    grads["conv_w"] = lax.dynamic_slice_in_dim(grads["conv_w"], chip * ch_shard, ch_shard, axis=1)
    d_shard = b_gate.shape[2]
    grads["b_gate"] = lax.dynamic_slice_in_dim(grads["b_gate"], chip * d_shard, d_shard, axis=1)

    grads["w_in"], delta["w_in"], new_m["w_in"], new_v["w_in"] = _adamw(
        w["w_in"][0], shard_w_in.reshape(-1, shard_w_in.shape[2]), m["w_in"][0], v["w_in"][0], "adamw_w_in")
    flat2 = lambda a: a.reshape(-1, a.shape[-1])
    d_s, m_s, v_s = _adamw_small([flat2(w[n]) for n in SMALL], [flat2(grads[n]) for n in SMALL],
                                 [flat2(m[n]) for n in SMALL], [flat2(v[n]) for n in SMALL], "adamw_small")
    for i, n in enumerate(SMALL):
        delta[n], new_m[n], new_v[n] = d_s[i], m_s[i], v_s[i]

    shaped = lambda d, n: d[n].reshape(w[n].shape)
    return (loss_total, grad_x[None], *[shaped(grads, n) for n in ORDER], *[shaped(delta, n) for n in ORDER],
            *[shaped(new_m, n) for n in ORDER], *[shaped(new_v, n) for n in ORDER])
```

```python
import functools

import numpy as np
import jax
import jax.numpy as jnp
from jax import lax
from jax.experimental import pallas as pl
from jax.experimental.pallas import tpu as pltpu

F32 = jnp.float32
BF16 = jnp.bfloat16
MESH = pl.DeviceIdType.MESH
ANY = pl.BlockSpec(memory_space=pl.ANY)

HEAD_DIM = 64
N_SLOT_HEADS = 8
DILATIONS = (1, 4, 16)
HALF_SPAN = 64
ROPE_THETA = 500000.0
ROT_DIM = 16
CONV_WIDTH = 31
EPS = 1e-6
NEG_INF = -1e30
ADAM_LR, ADAM_B1, ADAM_B2, ADAM_EPS, ADAM_WD, ADAM_STEP = 0.001, 0.9, 0.999, 1e-08, 0.01, 10

LANES = 128
QBLK = 128
KWIN = QBLK + 2 * HALF_SPAN
VMEM_LIMIT = 48 * 1024 * 1024
N_CHIPS = 4


def _params(**kw):
    return pltpu.CompilerParams(vmem_limit_bytes=VMEM_LIMIT, **kw)


class _Rider:
    def __init__(self, operands, out_shapes, aliases, scratch, start, wait, middle=None):
        self.operands, self.out_shapes, self.aliases = list(operands), list(out_shapes), dict(aliases)
        self.scratch, self.start, self.wait = list(scratch), start, wait
        self.middle = middle


def _riders_together(a, b):
    n_in, n_out, n_sc = len(a.operands), len(a.out_shapes), len(a.scratch)
    aliases = dict(a.aliases)
    aliases.update({n_in + src: n_out + dst for src, dst in b.aliases.items()})

    def start(r_in, r_out, r_sc):
        a.start(r_in[:n_in], r_out[:n_out], r_sc[:n_sc])
        b.start(r_in[n_in:], r_out[n_out:], r_sc[n_sc:])

    def wait(r_in, r_out, r_sc):
        a.wait(r_in[:n_in], r_out[:n_out], r_sc[:n_sc])
        b.wait(r_in[n_in:], r_out[n_out:], r_sc[n_sc:])

    return _Rider(a.operands + b.operands, a.out_shapes + b.out_shapes, aliases, a.scratch + b.scratch, start, wait)


def _pallas(body, *, name, grid, in_specs, out_specs, out_shape, operands, scratch_shapes=(), aliases=None,
            rider=None, after=()):
    single = not isinstance(out_specs, (list, tuple))
    out_specs_l = [out_specs] if single else list(out_specs)
    out_shape_l = [out_shape] if single else list(out_shape)
    aliases = dict(aliases or {})

    def call(fn, all_in_specs, all_out_specs, all_out_shape, all_scratch, all_aliases, all_operands):
        return pl.pallas_call(
            fn, name=name, grid=grid, in_specs=all_in_specs, out_specs=all_out_specs, out_shape=all_out_shape,
            scratch_shapes=all_scratch, input_output_aliases=all_aliases, compiler_params=_params(),
        )(*all_operands)

    if rider is None:
        n_main = len(in_specs)

        def ordered(*refs):
            body(*refs[:n_main], *refs[n_main + len(after):])

        res = call(ordered if after else body, list(in_specs) + [ANY] * len(after), out_specs_l, out_shape_l,
                   list(scratch_shapes), aliases, list(operands) + list(after))
        return res[0] if single else res
    assert not after
    n_in, n_rin = len(in_specs), len(rider.operands)
    n_out, n_rout = len(out_specs_l), len(rider.out_shapes)
    n_sc = len(scratch_shapes)

    def wrapped(*refs):
        main_in, r_in = refs[:n_in], refs[n_in:n_in + n_rin]
        o0 = n_in + n_rin
        main_out, r_out = refs[o0:o0 + n_out], refs[o0 + n_out:o0 + n_out + n_rout]
        s0 = o0 + n_out + n_rout
        main_sc, r_sc = refs[s0:s0 + n_sc], refs[s0 + n_sc:]
        ids = [pl.program_id(d) for d in range(len(grid))]
        first = functools.reduce(jnp.logical_and, [i == 0 for i in ids])
        last = functools.reduce(jnp.logical_and, [i == n - 1 for i, n in zip(ids, grid)])

        @pl.when(first)
        def _():
            rider.start(r_in, r_out, r_sc)

        body(*main_in, *main_out, *main_sc)

        @pl.when(last)
        def _():
            rider.wait(r_in, r_out, r_sc)

    for src, dst in rider.aliases.items():
        aliases[n_in + src] = n_out + dst
    res = call(wrapped, list(in_specs) + [ANY] * n_rin, out_specs_l + [ANY] * n_rout,
               out_shape_l + rider.out_shapes, list(scratch_shapes) + rider.scratch, aliases,
               list(operands) + rider.operands)
    main = res[:n_out]
    return (main[0] if single else main), res[n_out:]


def _matmul(a, b, *, mode, tm, tn, tk, out_dtype, name, b_blocked=False,
            out_blocked=None, cols_outer=False, rider=None, after=()):
    a_shape = a.shape
    if mode == "nn":
        m_dim, k_dim = a_shape
        n_dim = b.shape[0] * b.shape[2] if b_blocked else b.shape[1]
        rows, cols, red = m_dim, n_dim, k_dim
    elif mode == "nt":
        m_dim, n_dim = a_shape
        k_dim = b.shape[1] if b_blocked else b.shape[0]
        rows, cols, red = m_dim, k_dim, n_dim
    else:
        m_dim, k_dim = a_shape
        n_dim = b.shape[1]
        rows, cols, red = k_dim, n_dim, m_dim
    assert rows % tm == 0 and cols % tn == 0 and red % tk == 0, (name, rows, cols, red)
    ni, nj, nk = rows // tm, cols // tn, red // tk

    if mode == "nn":
        a_spec = pl.BlockSpec((tm, tk), lambda i, j, k: (i, k))
        if b_blocked:
            per = b.shape[2] // tn
            b_spec = pl.BlockSpec((None, tk, tn), lambda i, j, k: (j // per, k, j % per))
        else:
            b_spec = pl.BlockSpec((tk, tn), lambda i, j, k: (k, j))
        dims = (((1,), (0,)), ((), ()))
    elif mode == "nt":
        a_spec = pl.BlockSpec((tm, tk), lambda i, j, k: (i, k))
        if b_blocked:
            per = b.shape[2] // tk
            b_spec = pl.BlockSpec((None, tn, tk), lambda i, j, k: (k // per, j, k % per))
        else:
            b_spec = pl.BlockSpec((tn, tk), lambda i, j, k: (j, k))
        dims = (((1,), (1,)), ((), ()))
    else:
        a_spec = pl.BlockSpec((tk, tm), lambda i, j, k: (k, i))
        b_spec = pl.BlockSpec((tk, tn), lambda i, j, k: (k, j))
        dims = (((0,), (0,)), ((), ()))

    if out_blocked:
        per_o = (cols // out_blocked) // tn
        out_spec = pl.BlockSpec((None, tm, tn), lambda i, j, k: (j // per_o, i, j % per_o))
        out_shape = jax.ShapeDtypeStruct((out_blocked, rows, cols // out_blocked), out_dtype)
    else:
        out_spec = pl.BlockSpec((tm, tn), lambda i, j, k: (i, j))
        out_shape = jax.ShapeDtypeStruct((rows, cols), out_dtype)

    def body(a_ref, b_ref, o_ref, *acc):
        prod = lax.dot_general(a_ref[...], b_ref[...], dims, preferred_element_type=F32)
        if nk == 1:
            o_ref[...] = prod.astype(out_dtype)
        else:
            acc_ref, = acc
            k = pl.program_id(2)

            @pl.when(k == 0)
            def _():
                acc_ref[...] = prod

            @pl.when(k > 0)
            def _():
                acc_ref[...] += prod

            @pl.when(k == nk - 1)
            def _():
                o_ref[...] = acc_ref[...].astype(out_dtype)

    scratch = [pltpu.VMEM((tm, tn), F32)] if nk > 1 else []
    grid = (ni, nj, nk)
    if cols_outer:
        swap = lambda spec: pl.BlockSpec(spec.block_shape, lambda j, i, k, f=spec.index_map: f(i, j, k))
        a_spec, b_spec, out_spec, grid = swap(a_spec), swap(b_spec), swap(out_spec), (nj, ni, nk)
    return _pallas(body, name=name, grid=grid, in_specs=[a_spec, b_spec], out_specs=out_spec,
                   out_shape=out_shape, operands=[a, b], scratch_shapes=scratch, rider=rider, after=after)


def _proj_by_slot(h, w_in, order, first, count, name, proj=None, after=()):
    s, k = h.shape
    nb = w_in.shape[2]
    tm = 512
    prev = [] if proj is None else [proj]

    def body(order_ref, h_ref, w_ref, *refs):
        refs[-1][...] = jnp.dot(h_ref[...], w_ref[...], preferred_element_type=F32)

    return pl.pallas_call(
        body, name=name,
        grid_spec=pltpu.PrefetchScalarGridSpec(
            num_scalar_prefetch=1, grid=(count, s // tm),
            in_specs=[pl.BlockSpec((tm, k), lambda j, i, order_ref: (i, 0)),
                      pl.BlockSpec((None, k, nb), lambda j, i, order_ref: (order_ref[first + j], 0, 0))]
            + [ANY] * (len(prev) + len(after)),
            out_specs=pl.BlockSpec((tm, nb), lambda j, i, order_ref: (i, order_ref[first + j]))),
        out_shape=jax.ShapeDtypeStruct((s, N_CHIPS * nb), F32),
        input_output_aliases={3: 0} if prev else {}, compiler_params=_params(),
    )(order, h, w_in, *prev, *after)


def _rmsnorm_fwd(x, w, name):
    s, d = x.shape
    tm = 256

    def body(x_ref, w_ref, o_ref):
        xv = x_ref[...]
        rstd = lax.rsqrt(jnp.mean(xv * xv, axis=-1, keepdims=True) + EPS)
        o_ref[...] = (xv * rstd * w_ref[...]).astype(BF16)

    return pl.pallas_call(
        body, name=name, grid=(s // tm,),
        in_specs=[pl.BlockSpec((tm, d), lambda i: (i, 0)), pl.BlockSpec((1, d), lambda i: (0, 0))],
        out_specs=pl.BlockSpec((tm, d), lambda i: (i, 0)),
        out_shape=jax.ShapeDtypeStruct((s, d), BF16), compiler_params=_params(),
    )(x, w)


def _matmul_nt_rmsnorm_bwd(dz, w_blocked, x, w, dres, name, rider=None, after=()):
    s, d = x.shape
    nk, _, nb = w_blocked.shape
    tm = 512
    nt_dims = (((1,), (1,)), ((), ()))

    def body(a_ref, b_ref, x_ref, w_ref, dres_ref, dx_ref, dxb_ref, dw_ref, acc_ref):
        k, i = pl.program_id(0), pl.program_id(1)
        rows = pl.ds(pl.multiple_of(i * tm, tm), tm)
        prod = lax.dot_general(a_ref[...], b_ref[...], nt_dims, preferred_element_type=F32)

        @pl.when(k == 0)
        def _():
            acc_ref[rows, :] = prod

        @pl.when(jnp.logical_and(k > 0, k < nk - 1))
        def _():
            acc_ref[rows, :] += prod

        @pl.when(k == nk - 1)
        def _():
            xv = x_ref[...]
            rstd = lax.rsqrt(jnp.mean(xv * xv, axis=-1, keepdims=True) + EPS)
            xhat = xv * rstd
            dhv = acc_ref[rows, :] + prod
            g = dhv * w_ref[...]
            dx = rstd * (g - xhat * jnp.mean(g * xhat, axis=-1, keepdims=True)) + dres_ref[...]
            dx_ref[...] = dx
            dxb_ref[...] = dx.astype(BF16)
            part = jnp.sum(dhv * xhat, axis=0, keepdims=True)

            @pl.when(i == 0)
            def _():
                dw_ref[...] = part

            @pl.when(i > 0)
            def _():
                dw_ref[...] += part

    assert nk >= 2
    row = pl.BlockSpec((tm, d), lambda k, i: (jnp.where(k == nk - 1, i, 0), 0))
    vec = pl.BlockSpec((1, d), lambda k, i: (0, 0))
    return _pallas(
        body, name=name, grid=(nk, s // tm),
        in_specs=[pl.BlockSpec((tm, nb), lambda k, i: (i, k)), pl.BlockSpec((None, d, nb), lambda k, i: (k, 0, 0)),
                  row, vec, row],
        out_specs=[row, row, vec],
        out_shape=[jax.ShapeDtypeStruct((s, d), F32), jax.ShapeDtypeStruct((s, d), BF16),
                   jax.ShapeDtypeStruct((1, d), F32)],
        operands=[dz, w_blocked, x, w, dres], scratch_shapes=[pltpu.VMEM((s, d), F32)], rider=rider, after=after)


def _rope_consts():
    lane = np.arange(LANES)
    in_head = lane % HEAD_DIM
    inv_freq = ROPE_THETA ** (-jnp.arange(0, ROT_DIM, 2, dtype=F32) / ROT_DIM)
    invf = jnp.where(jnp.asarray(in_head < ROT_DIM), jnp.tile(inv_freq, LANES // (ROT_DIM // 2)), 0.0)
    m_a = np.where(in_head < ROT_DIM // 2, -1.0, 0.0).astype(np.float32)
    m_b = np.where((in_head >= ROT_DIM // 2) & (in_head < ROT_DIM), 1.0, 0.0).astype(np.float32)
    block_diag = (lane[:, None] // HEAD_DIM == lane[None, :] // HEAD_DIM).astype(np.float32)
    return (invf.reshape(1, LANES).astype(F32), jnp.asarray(m_a).reshape(1, LANES),
            jnp.asarray(m_b).reshape(1, LANES), jnp.asarray(block_diag, dtype=BF16))


def _head_sums(v, bd):
    hi = v.astype(BF16)
    lo = (v - hi.astype(F32)).astype(BF16)
    return jnp.dot(hi, bd, preferred_element_type=F32) + jnp.dot(lo, bd, preferred_element_type=F32)


def _rope_tables(pos_col, consts, name, after=()):
    s = pos_col.shape[0]
    tm = 512
    invf, m_a, m_b, _ = consts

    def body(pos_ref, invf_ref, ma_ref, mb_ref, cos_ref, sa_ref, sb_ref):
        ang = pos_ref[...].astype(F32) * invf_ref[...]
        sin = jnp.sin(ang)
        cos_ref[...] = jnp.cos(ang)
        sa_ref[...] = sin * ma_ref[...]
        sb_ref[...] = sin * mb_ref[...]

    vec = pl.BlockSpec((1, LANES), lambda i: (0, 0))
    tab = pl.BlockSpec((tm, LANES), lambda i: (i, 0))
    return _pallas(body, name=name, grid=(s // tm,), in_specs=[pl.BlockSpec((tm, 1), lambda i: (i, 0)), vec, vec, vec],
                   out_specs=[tab] * 3, out_shape=[jax.ShapeDtypeStruct((s, LANES), F32)] * 3,
                   operands=[pos_col, invf, m_a, m_b], after=after)


def _qk_fwd(proj, rope, qw2, kw2, bd, name, rider=None):
    s = proj.shape[0]
    width = 3 * N_SLOT_HEADS * HEAD_DIM
    tm = 128
    scale = HEAD_DIM ** -0.5

    def body(q_ref, k_ref, cos_ref, sa_ref, sb_ref, qw_ref, kw_ref, bd_ref, qo_ref, ko_ref):
        cos, s_a, s_b = cos_ref[...], sa_ref[...], sb_ref[...]
        bdv = bd_ref[...]
        for src, w_ref, dst, sc in ((q_ref, qw_ref, qo_ref, scale), (k_ref, kw_ref, ko_ref, 1.0)):
            for cb in range(width // LANES):
                cols = slice(cb * LANES, (cb + 1) * LANES)
                t = src[:, cols]
                rstd = lax.rsqrt(_head_sums(t * t, bdv) * (1.0 / HEAD_DIM) + EPS)
                y = t * rstd * w_ref[...]
                r = y * cos + pltpu.roll(y, LANES - 8, axis=1) * s_a + pltpu.roll(y, 8, axis=1) * s_b
                dst[:, cols] = r * sc if sc != 1.0 else r

    vec = pl.BlockSpec((1, LANES), lambda i: (0, 0))
    tab = pl.BlockSpec((tm, LANES), lambda i: (i, 0))
    return _pallas(
        body, name=name, grid=(s // tm,),
        in_specs=[pl.BlockSpec((tm, width), lambda i: (i, 0)), pl.BlockSpec((tm, width), lambda i: (i, 1)),
                  tab, tab, tab, vec, vec, pl.BlockSpec((LANES, LANES), lambda i: (0, 0))],
        out_specs=[pl.BlockSpec((tm, width), lambda i: (i, 0))] * 2,
        out_shape=[jax.ShapeDtypeStruct((s, width), F32)] * 2,
        operands=[proj, proj, *rope, qw2, kw2, bd], rider=rider)


def _qk_bwd(dqn, dkn, dv, da, db, dgl, proj, rope, qw2, kw2, bd, name, rider=None):
    s = proj.shape[0]
    width = 3 * N_SLOT_HEADS * HEAD_DIM
    ch = da.shape[1]
    gate_w = dgl.shape[1]
    out_w = 3 * width + 2 * ch + gate_w
    assert out_w == proj.shape[1]
    tm = 128
    scale = HEAD_DIM ** -0.5

    def body(dq_ref, dk_ref, dv_ref, da_ref, db_ref, dgl_ref, q_ref, k_ref, cos_ref, sa_ref, sb_ref, qw_ref, kw_ref,
             bd_ref, out_ref, dqw_ref, dkw_ref):
        cos, s_a, s_b = cos_ref[...], sa_ref[...], sb_ref[...]
        bdv = bd_ref[...]
        first = pl.program_id(0) == 0
        for src, dsrc, w_ref, col0, dw_ref, sc in ((q_ref, dq_ref, qw_ref, 0, dqw_ref, scale),
                                                   (k_ref, dk_ref, kw_ref, width, dkw_ref, 1.0)):
            dw_acc = jnp.zeros((1, LANES), F32)
            for cb in range(width // LANES):
                cols = slice(cb * LANES, (cb + 1) * LANES)
                t = src[:, cols]
                dr = dsrc[:, cols]
                if sc != 1.0:
                    dr = dr * sc
                dy = dr * cos + pltpu.roll(dr * s_a, 8, axis=1) + pltpu.roll(dr * s_b, LANES - 8, axis=1)
                rstd = lax.rsqrt(_head_sums(t * t, bdv) * (1.0 / HEAD_DIM) + EPS)
                xhat = t * rstd
                g = dy * w_ref[...]
                dt = rstd * (g - xhat * (_head_sums(g * xhat, bdv) * (1.0 / HEAD_DIM)))
                out_ref[:, col0 + cb * LANES: col0 + (cb + 1) * LANES] = dt.astype(BF16)
                dw_acc = dw_acc + jnp.sum(dy * xhat, axis=0, keepdims=True)
            dw_acc = dw_acc + pltpu.roll(dw_acc, HEAD_DIM, axis=1)

            @pl.when(first)
            def _(dw_ref=dw_ref, dw_acc=dw_acc):
                dw_ref[...] = dw_acc

            @pl.when(jnp.logical_not(first))
            def _(dw_ref=dw_ref, dw_acc=dw_acc):
                dw_ref[...] += dw_acc
        out_ref[:, 2 * width: 3 * width] = dv_ref[...].astype(BF16)
        out_ref[:, 3 * width: 3 * width + ch] = da_ref[...]
        out_ref[:, 3 * width + ch: 3 * width + 2 * ch] = db_ref[...]
        out_ref[:, 3 * width + 2 * ch: out_w] = dgl_ref[...]

    vec = pl.BlockSpec((1, LANES), lambda i: (0, 0))
    blk = lambda c: pl.BlockSpec((tm, width), lambda i: (i, c))
    cblk = pl.BlockSpec((tm, ch), lambda i: (i, 0))
    tab = pl.BlockSpec((tm, LANES), lambda i: (i, 0))
    return _pallas(
        body, name=name, grid=(s // tm,),
        in_specs=[blk(0), blk(0), blk(0), cblk, cblk, pl.BlockSpec((tm, gate_w), lambda i: (i, 0)),
                  blk(0), blk(1), tab, tab, tab, vec, vec, pl.BlockSpec((LANES, LANES), lambda i: (0, 0))],
        out_specs=[pl.BlockSpec((tm, out_w), lambda i: (i, 0)), vec, vec],
        out_shape=[jax.ShapeDtypeStruct((s, out_w), BF16)] + [jax.ShapeDtypeStruct((1, LANES), F32)] * 2,
        operands=[dqn, dkn, dv, da, db, dgl, proj, proj, *rope, qw2, kw2, bd], rider=rider)


def _row_chunks(n_rows, fn, chunk=256):
    def step(i, c):
        fn(pl.ds(pl.multiple_of(i * chunk, chunk), chunk))
        return c
    lax.fori_loop(0, n_rows // chunk, step, 0)


def _to_residue_major(dst, src, s, d, dst_off=0, cast=None):
    seq = s // d
    for r in range(d):
        v = src[...] if d == 1 else src[pl.ds(r, seq, stride=d), :]
        dst[dst_off + r * seq: dst_off + (r + 1) * seq, :] = v if cast is None else v.astype(cast)


def _from_residue_major(dst, src, s, d, src_off=0):
    seq = s // d
    for r in range(d):
        v = src[src_off + r * seq: src_off + (r + 1) * seq, :]
        if d == 1:
            dst[...] = v
        else:
            dst[pl.ds(r, seq, stride=d), :] = v


def _band_bias():
    qi = lax.broadcasted_iota(jnp.int32, (QBLK, KWIN), 0)
    kj = lax.broadcasted_iota(jnp.int32, (QBLK, KWIN), 1)
    return jnp.where(jnp.abs(kj - HALF_SPAN - qi) <= HALF_SPAN, 0.0, NEG_INF).astype(F32)


def _range_bias(base, seq):
    kj = lax.broadcasted_iota(jnp.int32, (1, KWIN), 1)
    lo = (base & -seq) - base + HALF_SPAN
    return jnp.where((kj >= lo) & (kj < lo + seq), 0.0, NEG_INF).astype(F32)


def _skewed_blocks(n_blk, produce, consume):
    produce(0, 0)
    for b in range(n_blk):
        consume(b, b % 2)
        if b + 1 < n_blk:
            produce(b + 1, (b + 1) % 2)


def _block_base(b):
    return b * QBLK if isinstance(b, int) else pl.multiple_of(b * QBLK, QBLK)


def _attn_fwd(qn, kn, proj, name, rider=None):
    s = qn.shape[0]
    n_pairs = N_SLOT_HEADS * HEAD_DIM // LANES
    v_col0 = 2 * qn.shape[1] // LANES
    nt_dims = (((1,), (1,)), ((), ()))

    def body(q_ref, k_ref, v_ref, attn_ref, lse_ref, attn_b_ref, q_rm, k_rm, v_rm, acc_rm, m_rm, l_rm,
             acc_p, m_p, l_p, m_run, l_run, acc_run, band, s_buf, m_buf):
        g = pl.program_id(1)
        zpad = jnp.zeros((HALF_SPAN, LANES), BF16)
        k_rm[0:HALF_SPAN, :] = zpad
        k_rm[s + HALF_SPAN: s + 2 * HALF_SPAN, :] = zpad
        v_rm[0:HALF_SPAN, 0:LANES] = zpad
        v_rm[s + HALF_SPAN: s + 2 * HALF_SPAN, 0:LANES] = zpad

        def ones_rows(rows):
            v_rm[pl.ds(rows.start, rows.size), LANES:2 * LANES] = jnp.ones((rows.size, LANES), BF16)

        _row_chunks(s + 2 * HALF_SPAN, ones_rows, chunk=2 * HALF_SPAN)
        band[...] = _band_bias()
        lane = lax.broadcasted_iota(jnp.int32, (QBLK, LANES), 1)
        low = lane < HEAD_DIM
        n_blk = s // QBLK

        for gi, d in enumerate(DILATIONS):
            @pl.when(g == gi)
            def _(gi=gi, d=d):
                seq = s // d
                _to_residue_major(q_rm, q_ref, s, d, cast=BF16)
                _to_residue_major(k_rm, k_ref, s, d, dst_off=HALF_SPAN, cast=BF16)
                _to_residue_major(v_rm.at[:, 0:LANES], v_ref, s, d, dst_off=HALF_SPAN, cast=BF16)

                def scores(b, slot):
                    base = _block_base(b)
                    q = q_rm[pl.ds(base, QBLK), :]
                    zero = jnp.zeros_like(q)
                    q2 = jnp.concatenate([jnp.where(low, q, zero), jnp.where(low, zero, q)], axis=0)
                    sc = lax.dot_general(q2, k_rm[pl.ds(base, KWIN), :], nt_dims, preferred_element_type=F32)
                    bias = band[...] + _range_bias(base, seq)
                    for hh in range(2):
                        rows = slice(hh * QBLK, (hh + 1) * QBLK)
                        sh = sc[rows, :] + bias
                        s_buf[slot, rows, :] = sh
                        m_buf[slot, rows, :] = jnp.broadcast_to(jnp.max(sh, axis=-1, keepdims=True), (QBLK, LANES))

                def outputs(b, slot):
                    base = _block_base(b)
                    sv = s_buf[slot]
                    mb = m_buf[slot]
                    p = jnp.exp(jnp.concatenate([sv[:, 0:LANES] - mb, sv[:, LANES:2 * LANES] - mb], axis=1))
                    pv = jnp.dot(p.astype(BF16), v_rm[pl.ds(base, KWIN), :], preferred_element_type=F32)
                    rows = pl.ds(base, QBLK)
                    acc_rm[rows, :] = jnp.where(low, pv[0:QBLK, 0:LANES], pv[QBLK:2 * QBLK, 0:LANES])
                    l_rm[rows, :] = jnp.where(low, pv[0:QBLK, LANES:2 * LANES], pv[QBLK:2 * QBLK, LANES:2 * LANES])
                    m_rm[rows, :] = jnp.where(low, mb[0:QBLK, :], mb[QBLK:2 * QBLK, :])

                _skewed_blocks(n_blk, scores, outputs)
                if d == 1:
                    src = (acc_rm, m_rm, l_rm)
                else:
                    for dst_, src_ in ((acc_p, acc_rm), (m_p, m_rm), (l_p, l_rm)):
                        _from_residue_major(dst_, src_, s, d)
                    src = (acc_p, m_p, l_p)

                def combine(rows):
                    a_g, m_g, l_g = src[0][rows, :], src[1][rows, :], src[2][rows, :]
                    if gi == 0:
                        m_new, l_new, a_new = m_g, l_g, a_g
                    else:
                        m_old = m_run[rows, :]
                        m_new = jnp.maximum(m_old, m_g)
                        w_old = jnp.exp(m_old - m_new)
                        w_g = jnp.exp(m_g - m_new)
                        l_new = l_run[rows, :] * w_old + l_g * w_g
                        a_new = acc_run[rows, :] * w_old + a_g * w_g
                    if gi == len(DILATIONS) - 1:
                        out = a_new / l_new
                        attn_ref[rows, :] = out
                        attn_b_ref[rows, :] = out.astype(BF16)
                        lse_ref[rows, :] = m_new + jnp.log(l_new)
                    else:
                        m_run[rows, :] = m_new
                        l_run[rows, :] = l_new
                        acc_run[rows, :] = a_new

                _row_chunks(s, combine)

    qk_spec = pl.BlockSpec((s, LANES), lambda hp, g: (0, g * n_pairs + hp))
    v_spec = pl.BlockSpec((s, LANES), lambda hp, g: (0, v_col0 + g * n_pairs + hp))
    o_spec = pl.BlockSpec((s, LANES), lambda hp, g: (0, hp))
    f32buf = pltpu.VMEM((s, LANES), F32)
    return _pallas(
        body, name=name, grid=(n_pairs, len(DILATIONS)), in_specs=[qk_spec, qk_spec, v_spec],
        out_specs=[o_spec, o_spec, o_spec],
        out_shape=[jax.ShapeDtypeStruct((s, n_pairs * LANES), F32)] * 2
        + [jax.ShapeDtypeStruct((s, n_pairs * LANES), BF16)],
        operands=[qn, kn, proj],
        scratch_shapes=[pltpu.VMEM((s, LANES), BF16), pltpu.VMEM((s + 2 * HALF_SPAN, LANES), BF16),
                        pltpu.VMEM((s + 2 * HALF_SPAN, 2 * LANES), BF16)] + [f32buf] * 9
        + [pltpu.VMEM((QBLK, KWIN), F32), pltpu.VMEM((2, 2 * QBLK, KWIN), F32),
           pltpu.VMEM((2, 2 * QBLK, LANES), F32)],
        rider=rider)


def _attn_bwd(qn, kn, proj, dattn, attn, lse, bd, name, rider=None):
    s = qn.shape[0]
    n_pairs = N_SLOT_HEADS * HEAD_DIM // LANES
    v_col0 = 2 * qn.shape[1] // LANES
    nt_dims = (((1,), (1,)), ((), ()))
    tn_dims = (((0,), (0,)), ((), ()))
    spad = s + 2 * HALF_SPAN

    def body(q_ref, k_ref, v_ref, do_ref, o_ref, lse_ref, bd_ref, dq_ref, dk_ref, dv_ref,
             q_rm, k_rm, v_rm, do_rm, lse0_rm, lse1_rm, dd0_rm, dd1_rm, dq_rm, dk_rm, dv_rm,
             lse0_p, lse1_p, dd0_p, dd1_p, band, p_buf, ds_buf):
        g = pl.program_id(1)
        zpad = jnp.zeros((HALF_SPAN, LANES), BF16)
        for buf in (k_rm, v_rm):
            buf[0:HALF_SPAN, :] = zpad
            buf[s + HALF_SPAN: spad, :] = zpad
        zf = jnp.zeros((HALF_SPAN, LANES), F32)
        for buf in (dk_rm, dv_rm):
            buf[0:HALF_SPAN, :] = zf
            buf[s + HALF_SPAN: spad, :] = zf
        band[...] = _band_bias()

        def clear(rows):
            z = jnp.zeros((rows.size, LANES), F32)
            dk_rm[pl.ds(rows.start + HALF_SPAN, rows.size), :] = z
            dv_rm[pl.ds(rows.start + HALF_SPAN, rows.size), :] = z

        _row_chunks(s, clear)

        def prepare(rows):
            lo = lax.broadcasted_iota(jnp.int32, (rows.size, LANES), 1) < HEAD_DIM
            dsum = _head_sums(do_ref[rows, :] * o_ref[rows, :], bd_ref[...])
            dswap = pltpu.roll(dsum, HEAD_DIM, axis=1)
            dd0_p[rows, :] = jnp.where(lo, dsum, dswap)
            dd1_p[rows, :] = jnp.where(lo, dswap, dsum)
            lv = lse_ref[rows, :]
            lswap = pltpu.roll(lv, HEAD_DIM, axis=1)
            lse0_p[rows, :] = jnp.where(lo, lv, lswap)
            lse1_p[rows, :] = jnp.where(lo, lswap, lv)

        @pl.when(g == 0)
        def _():
            _row_chunks(s, prepare)
        lane = lax.broadcasted_iota(jnp.int32, (QBLK, LANES), 1)
        low = lane < HEAD_DIM
        n_blk = s // QBLK

        def stacked(ref, rows):
            val = ref[rows, :]
            zero = jnp.zeros_like(val)
            return jnp.concatenate([jnp.where(low, val, zero), jnp.where(low, zero, val)], axis=0)

        for gi, d in enumerate(DILATIONS):
            @pl.when(g == gi)
            def _(d=d):
                seq = s // d
                _to_residue_major(q_rm, q_ref, s, d, cast=BF16)
                _to_residue_major(k_rm, k_ref, s, d, dst_off=HALF_SPAN, cast=BF16)
                _to_residue_major(v_rm, v_ref, s, d, dst_off=HALF_SPAN, cast=BF16)
                _to_residue_major(do_rm, do_ref, s, d, cast=BF16)
                for dst_, src_ in ((lse0_rm, lse0_p), (lse1_rm, lse1_p), (dd0_rm, dd0_p), (dd1_rm, dd1_p)):
                    _to_residue_major(dst_, src_, s, d)

                def scores(b, slot):
                    base = _block_base(b)
                    rows = pl.ds(base, QBLK)
                    win = pl.ds(base, KWIN)
                    sc = lax.dot_general(stacked(q_rm, rows), k_rm[win, :], nt_dims, preferred_element_type=F32)
                    dp = lax.dot_general(stacked(do_rm, rows), v_rm[win, :], nt_dims, preferred_element_type=F32)
                    bias = band[...] + _range_bias(base, seq)
                    for hh, (lse_r, dd_r) in enumerate(((lse0_rm, dd0_rm), (lse1_rm, dd1_rm))):
                        r = slice(hh * QBLK, (hh + 1) * QBLK)
                        lse_h = lse_r[rows, :]
                        dd_h = dd_r[rows, :]
                        sh = sc[r, :] + bias
                        p = jnp.exp(jnp.concatenate([sh[:, 0:LANES] - lse_h, sh[:, LANES:KWIN] - lse_h], axis=1))
                        dph = dp[r, :]
                        ds = p * jnp.concatenate([dph[:, 0:LANES] - dd_h, dph[:, LANES:KWIN] - dd_h], axis=1)
                        p_buf[slot, r, :] = p.astype(BF16)
                        ds_buf[slot, r, :] = ds.astype(BF16)

                def grads(b, slot):
                    base = _block_base(b)
                    rows = pl.ds(base, QBLK)
                    win = pl.ds(base, KWIN)
                    p = p_buf[slot]
                    ds = ds_buf[slot]
                    dq2 = jnp.dot(ds, k_rm[win, :], preferred_element_type=F32)
                    dq_rm[rows, :] = jnp.where(low, dq2[0:QBLK, :], dq2[QBLK:2 * QBLK, :])
                    dk_rm[win, :] += lax.dot_general(ds, stacked(q_rm, rows), tn_dims, preferred_element_type=F32)
                    dv_rm[win, :] += lax.dot_general(p, stacked(do_rm, rows), tn_dims, preferred_element_type=F32)

                _skewed_blocks(n_blk, scores, grads)
                _from_residue_major(dq_ref, dq_rm, s, d)
                _from_residue_major(dk_ref, dk_rm, s, d, src_off=HALF_SPAN)
                _from_residue_major(dv_ref, dv_rm, s, d, src_off=HALF_SPAN)

    qk_spec = pl.BlockSpec((s, LANES), lambda hp, g: (0, g * n_pairs + hp))
    v_spec = pl.BlockSpec((s, LANES), lambda hp, g: (0, v_col0 + g * n_pairs + hp))
    o_spec = pl.BlockSpec((s, LANES), lambda hp, g: (0, hp))
    width = qn.shape[1]
    f32buf = pltpu.VMEM((s, LANES), F32)
    f32pad = pltpu.VMEM((spad, LANES), F32)
    return _pallas(
        body, name=name, grid=(n_pairs, len(DILATIONS)),
        in_specs=[qk_spec, qk_spec, v_spec, o_spec, o_spec, o_spec,
                  pl.BlockSpec((LANES, LANES), lambda hp, g: (0, 0))],
        out_specs=[qk_spec, qk_spec, qk_spec],
        out_shape=[jax.ShapeDtypeStruct((s, width), F32)] * 3,
        operands=[qn, kn, proj, dattn, attn, lse, bd],
        scratch_shapes=[pltpu.VMEM((s, LANES), BF16), pltpu.VMEM((spad, LANES), BF16),
                        pltpu.VMEM((spad, LANES), BF16), pltpu.VMEM((s, LANES), BF16),
                        f32buf, f32buf, f32buf, f32buf, f32buf, f32pad, f32pad,
                        f32buf, f32buf, f32buf, f32buf, pltpu.VMEM((QBLK, KWIN), F32),
                        pltpu.VMEM((2, 2 * QBLK, KWIN), BF16), pltpu.VMEM((2, 2 * QBLK, KWIN), BF16)],
        rider=rider)


CONV_PAD = 16


def _conv_fwd(proj, conv_w, conv_b, col0, name, rider=None):
    s = proj.shape[0]
    ch = conv_w.shape[1]
    nblk = ch // LANES
    a0 = col0 // LANES
    tr = 256
    shift = CONV_PAD - (CONV_WIDTH - 1) // 2

    def body(a_ref, b_ref, w_ref, bias_ref, u0_ref, uc_ref, pad):
        z = jnp.zeros((CONV_PAD, LANES), F32)
        pad[0:CONV_PAD, :] = z
        pad[s + CONV_PAD: s + 2 * CONV_PAD, :] = z

        def glu(rows):
            u0 = a_ref[rows, :] * jax.nn.sigmoid(b_ref[rows, :])
            u0_ref[rows, :] = u0
            pad[pl.ds(rows.start + CONV_PAD, rows.size), :] = u0

        _row_chunks(s, glu)
        for t in range(0, s, tr):
            acc = jnp.broadcast_to(bias_ref[...], (tr, LANES))
            for k in range(CONV_WIDTH):
                acc = acc + w_ref[k:k + 1, :] * pad[t + k + shift: t + k + shift + tr, :]
            uc_ref[t:t + tr, :] = acc

    return _pallas(
        body, name=name, grid=(nblk,),
        in_specs=[pl.BlockSpec((s, LANES), lambda c: (0, a0 + c)),
                  pl.BlockSpec((s, LANES), lambda c: (0, a0 + nblk + c)),
                  pl.BlockSpec((CONV_WIDTH, LANES), lambda c: (0, c)),
                  pl.BlockSpec((1, LANES), lambda c: (0, c))],
        out_specs=[pl.BlockSpec((s, LANES), lambda c: (0, c))] * 2,
        out_shape=[jax.ShapeDtypeStruct((s, ch), F32)] * 2, operands=[proj, proj, conv_w, conv_b],
        scratch_shapes=[pltpu.VMEM((s + 2 * CONV_PAD, LANES), F32)], rider=rider)


def _ln_silu_fwd(uc, ln_w, ln_b, name):
    s, ch = uc.shape
    tm = 256

    def body(u_ref, w_ref, b_ref, o_ref):
        u = u_ref[...]
        mu = jnp.mean(u, axis=-1, keepdims=True)
        xc = u - mu
        rstd = lax.rsqrt(jnp.mean(xc * xc, axis=-1, keepdims=True) + EPS)
        z = xc * rstd * w_ref[...] + b_ref[...]
        o_ref[...] = (z * jax.nn.sigmoid(z)).astype(BF16)

    row = pl.BlockSpec((tm, ch), lambda i: (i, 0))
    vec = pl.BlockSpec((1, ch), lambda i: (0, 0))
    return pl.pallas_call(
        body, name=name, grid=(s // tm,), in_specs=[row, vec, vec], out_specs=row,
        out_shape=jax.ShapeDtypeStruct((s, ch), BF16), compiler_params=_params(),
    )(uc, ln_w, ln_b)


def _ln_silu_bwd(du3, uc, ln_w, ln_b, name):
    s, ch = uc.shape
    tm = 256

    def body(d_ref, u_ref, w_ref, b_ref, du_ref, dw_ref, db_ref):
        u = u_ref[...]
        mu = jnp.mean(u, axis=-1, keepdims=True)
        xc = u - mu
        rstd = lax.rsqrt(jnp.mean(xc * xc, axis=-1, keepdims=True) + EPS)
        xhat = xc * rstd
        z = xhat * w_ref[...] + b_ref[...]
        sg = jax.nn.sigmoid(z)
        dz = d_ref[...] * (sg * (1.0 + z * (1.0 - sg)))
        dxh = dz * w_ref[...]
        du_ref[...] = rstd * (dxh - jnp.mean(dxh, axis=-1, keepdims=True)
                              - xhat * jnp.mean(dxh * xhat, axis=-1, keepdims=True))
        pw = jnp.sum(dz * xhat, axis=0, keepdims=True)
        pb = jnp.sum(dz, axis=0, keepdims=True)
        first = pl.program_id(0) == 0

        @pl.when(first)
        def _():
            dw_ref[...] = pw
            db_ref[...] = pb

        @pl.when(jnp.logical_not(first))
        def _():
            dw_ref[...] += pw
            db_ref[...] += pb

    row = pl.BlockSpec((tm, ch), lambda i: (i, 0))
    vec = pl.BlockSpec((1, ch), lambda i: (0, 0))
    return pl.pallas_call(
        body, name=name, grid=(s // tm,), in_specs=[row, row, vec, vec], out_specs=[row, vec, vec],
        out_shape=[jax.ShapeDtypeStruct((s, ch), F32), jax.ShapeDtypeStruct((1, ch), F32),
                   jax.ShapeDtypeStruct((1, ch), F32)],
        compiler_params=_params(),
    )(du3, uc, ln_w, ln_b)


def _conv_bwd(duc, u0, proj, conv_w, col0, name, rider=None):
    s = proj.shape[0]
    ch = conv_w.shape[1]
    nblk = ch // LANES
    a0 = col0 // LANES
    tr = 256
    half = (CONV_WIDTH - 1) // 2
    shift = CONV_PAD - half

    def body(duc_ref, u0_ref, a_ref, b_ref, w_ref, da_ref, db_ref, dw_ref, dbias_ref, pad_d, pad_u):
        z = jnp.zeros((CONV_PAD, LANES), F32)
        for buf in (pad_d, pad_u):
            buf[0:CONV_PAD, :] = z
            buf[s + CONV_PAD: s + 2 * CONV_PAD, :] = z

        def fill(rows):
            dst = pl.ds(rows.start + CONV_PAD, rows.size)
            pad_d[dst, :] = duc_ref[rows, :]
            pad_u[dst, :] = u0_ref[rows, :]

        _row_chunks(s, fill)
        dw_acc = [jnp.zeros((8, LANES), F32) for _ in range(CONV_WIDTH)]
        dbias_acc = jnp.zeros((8, LANES), F32)
        for t in range(0, s, tr):
            d_t = duc_ref[t:t + tr, :]
            dbias_acc = dbias_acc + jnp.sum(d_t.reshape(tr // 8, 8, LANES), axis=0)
            du0 = jnp.zeros((tr, LANES), F32)
            for k in range(CONV_WIDTH):
                du0 = du0 + w_ref[k:k + 1, :] * pad_d[t - k + half + CONV_PAD: t - k + half + CONV_PAD + tr, :]
                prod = d_t * pad_u[t + k + shift: t + k + shift + tr, :]
                dw_acc[k] = dw_acc[k] + jnp.sum(prod.reshape(tr // 8, 8, LANES), axis=0)
            av = a_ref[t:t + tr, :]
            sg = jax.nn.sigmoid(b_ref[t:t + tr, :])
            da_ref[t:t + tr, :] = (du0 * sg).astype(BF16)
            db_ref[t:t + tr, :] = (du0 * av * sg * (1.0 - sg)).astype(BF16)
        for k in range(CONV_WIDTH):
            dw_ref[k:k + 1, :] = jnp.sum(dw_acc[k], axis=0, keepdims=True)
        dbias_ref[...] = jnp.sum(dbias_acc, axis=0, keepdims=True)

    col = lambda off: pl.BlockSpec((s, LANES), lambda c: (0, off + c))
    return _pallas(
        body, name=name, grid=(nblk,),
        in_specs=[col(0), col(0), col(a0), col(a0 + nblk),
                  pl.BlockSpec((CONV_WIDTH, LANES), lambda c: (0, c))],
        out_specs=[col(0), col(0), pl.BlockSpec((CONV_WIDTH, LANES), lambda c: (0, c)),
                   pl.BlockSpec((1, LANES), lambda c: (0, c))],
        out_shape=[jax.ShapeDtypeStruct((s, ch), BF16)] * 2
        + [jax.ShapeDtypeStruct((CONV_WIDTH, ch), F32), jax.ShapeDtypeStruct((1, ch), F32)],
        operands=[duc, u0, proj, proj, conv_w],
        scratch_shapes=[pltpu.VMEM((s + 2 * CONV_PAD, LANES), F32)] * 2, rider=rider)


def _mix_out_proj(attn_b, u3, w_o, w_pw, proj, bg, col0, w_out, x, norm_w, name, rider=None):
    s, d = x.shape
    k = attn_b.shape[1]
    tm = 256
    half = d // 2
    assert col0 % half == 0
    c0 = col0 // half

    def body(a_ref, u_ref, wo_ref, wp_ref, a0_ref, a1_ref, b0_ref, b1_ref, bias_ref, w_ref, x_ref, nw_ref,
             ya_ref, yb_ref, mixed_ref, x1_ref, h2_ref):
        mixed = None
        for br, (src_ref, wb_ref, lo_ref, hi_ref, y_ref) in enumerate(((a_ref, wo_ref, a0_ref, a1_ref, ya_ref),
                                                                       (u_ref, wp_ref, b0_ref, b1_ref, yb_ref))):
            src = src_ref[...]
            y = jnp.concatenate([jnp.dot(src, wb_ref[j], preferred_element_type=F32) for j in range(N_CHIPS)],
                                axis=1)
            y_ref[...] = y
            logits = jnp.concatenate([lo_ref[...], hi_ref[...]], axis=1)
            part = jax.nn.sigmoid(logits + bias_ref[br]) * y
            mixed = part if mixed is None else mixed + part
        mixed = mixed.astype(BF16)
        mixed_ref[...] = mixed
        x1 = x_ref[...] + jnp.dot(mixed, w_ref[...], preferred_element_type=F32)
        x1_ref[...] = x1
        rstd = lax.rsqrt(jnp.mean(x1 * x1, axis=-1, keepdims=True) + EPS)
        h2_ref[...] = (x1 * rstd * nw_ref[...]).astype(BF16)

    row = pl.BlockSpec((tm, d), lambda i: (i, 0))
    src_row = pl.BlockSpec((tm, k), lambda i: (i, 0))
    blocks = pl.BlockSpec(w_o.shape, lambda i: (0, 0, 0))
    logit_blk = lambda j: pl.BlockSpec((tm, half), functools.partial(lambda i, j: (i, c0 + j), j=j))
    return _pallas(
        body, name=name, grid=(s // tm,),
        in_specs=[src_row, src_row, blocks, blocks, logit_blk(0), logit_blk(1), logit_blk(2), logit_blk(3),
                  pl.BlockSpec((2, 1, d), lambda i: (0, 0, 0)), pl.BlockSpec((d, d), lambda i: (0, 0)), row,
                  pl.BlockSpec((1, d), lambda i: (0, 0))],
        out_specs=[row] * 5,
        out_shape=[jax.ShapeDtypeStruct((s, d), F32), jax.ShapeDtypeStruct((s, d), F32),
                   jax.ShapeDtypeStruct((s, d), BF16), jax.ShapeDtypeStruct((s, d), F32),
                   jax.ShapeDtypeStruct((s, d), BF16)],
        operands=[attn_b, u3, w_o, w_pw, proj, proj, proj, proj, bg, w_out, x, norm_w], rider=rider)


def _out_proj_bwd_gates(dx1, w_out, proj, bg, y_a, y_b, w_o, w_pw, col0, name, after=()):
    s, d = y_a.shape
    n_blk, k, blk = w_o.shape
    tm = 256
    half = d // 2
    assert col0 % half == 0
    c0 = col0 // half
    nt_dims = (((1,), (1,)), ((), ()))

    def body(dx_ref, w_ref, a0_ref, a1_ref, b0_ref, b1_ref, bias_ref, ya_ref, yb_ref, wo_ref, wp_ref,
             dgl_ref, dya_ref, dyb_ref, db_ref, da_ref, du_ref):
        dm = lax.dot_general(dx_ref[...], w_ref[...], nt_dims, preferred_element_type=F32)
        parts = []
        for br, (lo_ref, hi_ref, y_ref, dy_ref, wb_ref, dsrc_ref) in enumerate((
                (a0_ref, a1_ref, ya_ref, dya_ref, wo_ref, da_ref), (b0_ref, b1_ref, yb_ref, dyb_ref, wp_ref, du_ref))):
            logits = jnp.concatenate([lo_ref[...], hi_ref[...]], axis=1)
            gate = jax.nn.sigmoid(logits + bias_ref[br])
            dy = (dm * gate).astype(BF16)
            dy_ref[...] = dy
            dsrc = lax.dot_general(dy[:, 0:blk], wb_ref[0], nt_dims, preferred_element_type=F32)
            for j in range(1, n_blk):
                dsrc = dsrc + lax.dot_general(dy[:, j * blk:(j + 1) * blk], wb_ref[j], nt_dims,
                                              preferred_element_type=F32)
            dsrc_ref[...] = dsrc
            dgl = dm * y_ref[...] * gate * (1.0 - gate)
            dgl_ref[:, br * d:(br + 1) * d] = dgl.astype(BF16)
            parts.append(jnp.sum(dgl, axis=0, keepdims=True))
        part = jnp.concatenate(parts, axis=0)
        first = pl.program_id(0) == 0

        @pl.when(first)
        def _():
            db_ref[...] = part

        @pl.when(jnp.logical_not(first))
        def _():
            db_ref[...] += part

    row = pl.BlockSpec((tm, d), lambda i: (i, 0))
    logit_blk = lambda k: pl.BlockSpec((tm, half), functools.partial(lambda i, k: (i, c0 + k), k=k))
    blocks = pl.BlockSpec(w_o.shape, lambda i: (0, 0, 0))
    src_row = pl.BlockSpec((tm, k), lambda i: (i, 0))
    return _pallas(
        body, name=name, grid=(s // tm,),
        in_specs=[row, pl.BlockSpec((d, d), lambda i: (0, 0)), logit_blk(0), logit_blk(1), logit_blk(2),
                  logit_blk(3), pl.BlockSpec((2, 1, d), lambda i: (0, 0, 0)), row, row, blocks, blocks],
        out_specs=[pl.BlockSpec((tm, 2 * d), lambda i: (i, 0)), row, row, pl.BlockSpec((2, d), lambda i: (0, 0)),
                   src_row, src_row],
        out_shape=[jax.ShapeDtypeStruct((s, 2 * d), BF16), jax.ShapeDtypeStruct((s, d), BF16),
                   jax.ShapeDtypeStruct((s, d), BF16), jax.ShapeDtypeStruct((2, d), F32),
                   jax.ShapeDtypeStruct((s, k), F32), jax.ShapeDtypeStruct((s, k), F32)],
        operands=[dx1, w_out, proj, proj, proj, proj, bg, y_a, y_b, w_o, w_pw], after=after)


def _ffn_in_swiglu(h2, w_blocked, name):
    s, k = h2.shape
    nblk, _, tn = w_blocked.shape
    ff = nblk // 2 * tn
    tm = 512

    def body(a_ref, wg_ref, wu_ref, g_ref, u_ref, act_ref):
        a = a_ref[...]
        gt = jnp.dot(a, wg_ref[...], preferred_element_type=F32)
        up = jnp.dot(a, wu_ref[...], preferred_element_type=F32)
        g_ref[...] = gt
        u_ref[...] = up
        act_ref[...] = (gt * jax.nn.sigmoid(gt) * up).astype(BF16)

    out = pl.BlockSpec((tm, tn), lambda j, i: (i, j))
    return pl.pallas_call(
        body, name=name, grid=(nblk // 2, s // tm),
        in_specs=[pl.BlockSpec((tm, k), lambda j, i: (i, 0)),
                  pl.BlockSpec((None, k, tn), lambda j, i: (j, 0, 0)),
                  pl.BlockSpec((None, k, tn), lambda j, i: (nblk // 2 + j, 0, 0))],
        out_specs=[out, out, out],
        out_shape=[jax.ShapeDtypeStruct((s, ff), F32), jax.ShapeDtypeStruct((s, ff), F32),
                   jax.ShapeDtypeStruct((s, ff), BF16)],
        compiler_params=_params(),
    )(h2, w_blocked, w_blocked)


def _ffn_out_bwd_swiglu(dy, w_ffn_out, gate, up, name, rider=None):
    s, d = dy.shape
    ff = gate.shape[1]
    tm = 256
    nt_dims = (((1,), (1,)), ((), ()))

    def body(dy_ref, w_ref, g_ref, u_ref, o_ref):
        dv = lax.dot_general(dy_ref[...], w_ref[...], nt_dims, preferred_element_type=F32)
        gt = g_ref[...]
        sg = jax.nn.sigmoid(gt)
        o_ref[:, 0:ff] = (dv * u_ref[...] * (sg * (1.0 + gt * (1.0 - sg)))).astype(BF16)
        o_ref[:, ff:2 * ff] = (dv * gt * sg).astype(BF16)

    row = pl.BlockSpec((tm, ff), lambda i: (i, 0))
    return _pallas(
        body, name=name, grid=(s // tm,),
        in_specs=[pl.BlockSpec((tm, d), lambda i: (i, 0)), pl.BlockSpec((ff, d), lambda i: (0, 0)), row, row],
        out_specs=pl.BlockSpec((tm, 2 * ff), lambda i: (i, 0)),
        out_shape=jax.ShapeDtypeStruct((s, 2 * ff), BF16), operands=[dy, w_ffn_out, gate, up], rider=rider)


def _ffn_out_loss(act, w_ffn_out, x1, target, name):
    s, k = act.shape
    d = w_ffn_out.shape[1]
    tm = 512

    def body(a_ref, w_ref, x1_ref, t_ref, dy_ref, dyb_ref, loss_ref, acc):
        y = x1_ref[...] + jnp.dot(a_ref[...], w_ref[...], preferred_element_type=F32)
        diff = y - t_ref[...]
        dy = diff * (1.0 / d)
        dy_ref[...] = dy
        dyb_ref[...] = dy.astype(BF16)
        part = jnp.sum((diff * diff).reshape(tm // 8, 8, d), axis=0)
        i = pl.program_id(0)

        @pl.when(i == 0)
        def _():
            acc[...] = part

        @pl.when(i > 0)
        def _():
            acc[...] += part

        @pl.when(i == pl.num_programs(0) - 1)
        def _():
            loss_ref[...] = (0.5 / d) * jnp.sum(jnp.sum(acc[...], axis=1, keepdims=True), axis=0, keepdims=True)

    row = pl.BlockSpec((tm, d), lambda i: (i, 0))
    return pl.pallas_call(
        body, name=name, grid=(s // tm,),
        in_specs=[pl.BlockSpec((tm, k), lambda i: (i, 0)), pl.BlockSpec((k, d), lambda i: (0, 0)), row, row],
        out_specs=[row, row, pl.BlockSpec((1, 1), lambda i: (0, 0))],
        out_shape=[jax.ShapeDtypeStruct((s, d), F32), jax.ShapeDtypeStruct((s, d), BF16),
                   jax.ShapeDtypeStruct((1, 1), F32)],
        scratch_shapes=[pltpu.VMEM((8, d), F32)], compiler_params=_params(),
    )(act, w_ffn_out, x1, target)


LATE_GATHER = ("w_o_attn", "w_pw_conv", "w_out", "w_ffn_in", "w_ffn_out")
EARLY_REDUCE = LATE_GATHER


def _blocks_by_half(g):
    if g.ndim == 2:
        g = g.reshape(N_CHIPS, g.shape[0] // N_CHIPS, g.shape[1])
    return g.reshape(N_CHIPS, 2, g.shape[1] // 2, g.shape[2])


def _forward_backward(x, pos_col, target, wts, first_gather, late_bufs, pos_arr):
    wts = dict(wts)
    consts = _rope_consts()
    bd = consts[3]
    qw2 = jnp.tile(wts["q_norm_w"], (1, LANES // HEAD_DIM))
    kw2 = jnp.tile(wts["k_norm_w"], (1, LANES // HEAD_DIM))
    qkv_w = 3 * N_SLOT_HEADS * HEAD_DIM
    conv_col0 = 3 * qkv_w

    h = _rmsnorm_fwd(x, wts["norm1_w"], "rms1_fwd")
    slot_order = jnp.bitwise_xor(pos_arr[1], jnp.asarray([0, 2, 1, 3], jnp.int32))
    blocked = lambda buf: buf.reshape(N_CHIPS, -1, buf.shape[3])
    near = first_gather
    rope = _rope_tables(pos_col, consts, "rope_tables", after=list(late_bufs))
    proj = _proj_by_slot(h, blocked(near[2][0]), slot_order, 0, 1, "mm_proj_own", after=list(rope))
    near = _split_middle(near, after=[proj], name="allgather_w_in_near_forward")
    far, _ = _split_start(_gather_both_legs_rider(near[2][:1], near[2][1:], FAR_CHIPS), "allgather_w_in_far_start")
    _, bufs = _split_wait((near[0], near[1], far[2], near[3]), after=[], name="allgather_w_in_near_wait")
    proj = _proj_by_slot(h, blocked(bufs[0]), slot_order, 1, len(NEAR_CHIPS), "mm_proj_near", proj=proj)
    far = _split_middle((far[0], far[1], bufs, far[3]), after=[proj], name="allgather_w_in_far_forward")
    (w_in_buf, conv_w_buf, b_gate_buf), _ = _split_wait(far, after=[], name="allgather_w_in_far_wait")
    wts["w_in"] = w_in_buf.reshape(N_CHIPS, -1, w_in_buf.shape[3])
    wts["conv_w"] = conv_w_buf.transpose(1, 0, 2).reshape(CONV_WIDTH, -1)
    wts["b_gate"] = b_gate_buf.transpose(1, 0, 2).reshape(2, 1, -1)
    ch = wts["conv_w"].shape[1]
    gate_col0 = conv_col0 + 2 * ch
    n_mix = LATE_GATHER.index("w_ffn_in")
    mix_rider, ffn_rider = _gather_ici_rider(late_bufs[:n_mix], []), _gather_ici_rider(late_bufs[n_mix:], [])
    both, started = _split_start(_riders_together(mix_rider, ffn_rider), "late_gather_start", after=[wts["w_in"]])
    n_sem, n_buf = len(mix_rider.scratch), len(mix_rider.operands)
    mix_gather = (mix_rider, both[1][:n_sem], both[2][:n_buf], [])
    ffn_gather = (ffn_rider, both[1][n_sem:], both[2][n_buf:], [])
    proj = _proj_by_slot(h, wts["w_in"], slot_order, 1 + len(NEAR_CHIPS), len(FAR_CHIPS), "mm_proj_far", proj=proj,
                         after=[started])
    qn, kn = _qk_fwd(proj, rope, qw2, kw2, bd, "qk_fwd")
    attn, lse, attn_b = _attn_fwd(qn, kn, proj, "attn_fwd")

    def gathered(names, bufs):
        for n, buf in zip(names, bufs):
            full = buf.reshape(N_CHIPS, -1, buf.shape[3])
            wts[n] = full.reshape(-1, full.shape[2]) if n in ROW_SHARDED else full

    mix_bufs, _ = _split_wait(mix_gather, after=[attn_b], name="late_gather_mix_wait")
    (u0, uc), mix_bufs = _conv_fwd(proj, wts["conv_w"], wts["conv_b"], conv_col0, "conv_fwd",
                                   rider=_gather_forward_rider(mix_bufs))
    gathered(LATE_GATHER[:n_mix], mix_bufs)
    u3 = _ln_silu_fwd(uc, wts["conv_ln_w"], wts["conv_ln_b"], "ln_fwd")
    ffn_bufs, _ = _split_wait(ffn_gather, after=[u3], name="late_gather_ffn_wait")
    (y_a, y_b, mixed, x1, h2), ffn_bufs = _mix_out_proj(
        attn_b, u3, wts["w_o_attn"], wts["w_pw_conv"], proj, wts["b_gate"], gate_col0, wts["w_out"], x,
        wts["norm2_w"], "mix_x1_rms2", rider=_gather_forward_rider(ffn_bufs))
    gathered(LATE_GATHER[n_mix:], ffn_bufs)
    gate, up, act = _ffn_in_swiglu(h2, wts["w_ffn_in"], "mm_gu_swiglu")
    dy, dy_b16, loss = _ffn_out_loss(act, wts["w_ffn_out"], x1, target, "mm_x2_loss")

    g = {}
    by_chip = {}

    def pair_add(n, blocks, received):
        return _add_own_half(blocks, received, pos_arr, f"grads_pair_add_{n}")

    g_ffn_out = _blocks_by_half(
        _matmul(act, dy_b16, mode="tn", tm=1408, tn=1024, tk=2048, out_dtype=F32, name="mm_dwffnout"))
    dgu, (received,) = _ffn_out_bwd_swiglu(dy_b16, wts["w_ffn_out"], gate, up, "mm_dact_swiglu_bwd",
                                           rider=_pair_exchange_rider([g_ffn_out], halved=True))
    to_send, own = pair_add("w_ffn_out", g_ffn_out, received)
    (dx1, dx1_b16, g["norm2_w"]), (by_chip["w_ffn_out"],) = _matmul_nt_rmsnorm_bwd(
        dgu, wts["w_ffn_in"], x1, wts["norm2_w"], dy, "mm_dh2_rms2_bwd", rider=_chip_exchange_rider([to_send], [own]))
    g_ffn_in = _blocks_by_half(_matmul(h2, dgu, mode="tn", tm=512, tn=1408, tk=2048, out_dtype=F32,
                                       name="mm_dwffnin", out_blocked=N_CHIPS, cols_outer=True))
    exchanging, started = _split_start(_pair_exchange_rider([g_ffn_in], halved=True), "grads_ffn_in_pair_start")
    g["w_out"] = _matmul(mixed, dx1_b16, mode="tn", tm=512, tn=1024, tk=2048, out_dtype=F32, name="mm_dwout",
                         after=[started])
    dgl, dy_a, dy_b, g["b_gate"], dattn, du3 = _out_proj_bwd_gates(
        dx1_b16, wts["w_out"], proj, wts["b_gate"], y_a, y_b, wts["w_o_attn"], wts["w_pw_conv"], gate_col0,
        "mix_bwd")
    (received,), (g_ffn_in,) = _split_wait(exchanging, after=[dgl], name="grads_ffn_in_pair_wait")
    ffn_in_to_send, ffn_in_own = pair_add("w_ffn_in", g_ffn_in, received)
    g["w_o_attn"] = _matmul(attn_b, dy_a, mode="tn", tm=512, tn=256, tk=2048, out_dtype=F32, name="mm_dwo",
                            out_blocked=N_CHIPS)
    g["w_pw_conv"] = _matmul(u3, dy_b, mode="tn", tm=512, tn=256, tk=2048, out_dtype=F32, name="mm_dwpw",
                             out_blocked=N_CHIPS)
    duc, g["conv_ln_w"], g["conv_ln_b"] = _ln_silu_bwd(du3, uc, wts["conv_ln_w"], wts["conv_ln_b"], "ln_bwd")

    small3 = ("w_out", "w_o_attn", "w_pw_conv")
    g_small3 = [_blocks_by_half(g.pop(n)) for n in small3]
    (da, db, g["conv_w"], g["conv_b"]), received = _conv_bwd(
        duc, u0, proj, wts["conv_w"], conv_col0, "conv_bwd", rider=_pair_exchange_rider(g_small3, halved=True))
    sums3 = [pair_add(n, gb, rv) for n, gb, rv in zip(small3, g_small3, received)]
    (dqn, dkn, dv), (by_chip["w_ffn_in"],) = _attn_bwd(
        qn, kn, proj, dattn, attn, lse, bd, "attn_bwd",
        rider=_chip_exchange_rider([ffn_in_to_send], [ffn_in_own]))
    (dproj, dqw, dkw), exchanged3 = _qk_bwd(
        dqn, dkn, dv, da, db, dgl, proj, rope, qw2, kw2, bd, "qk_bwd",
        rider=_chip_exchange_rider([s[0] for s in sums3], [s[1] for s in sums3]))
    by_chip.update(zip(small3, exchanged3))
    halves = [_sum_chips(by_chip[n], pos_arr, f"grads_chip_sum_{n}") for n in EARLY_REDUCE]
    g["q_norm_w"] = dqw[:, :HEAD_DIM]
    g["k_norm_w"] = dkw[:, :HEAD_DIM]

    c = pos_arr[0]
    rh = h.shape[1] // 2
    h_sibling = lax.dynamic_slice_in_dim(h, (1 - c) * rh, rh, axis=1)
    h_own = lax.dynamic_slice_in_dim(h, c * rh, rh, axis=1)
    g_sibling, shards = _matmul(h_sibling, dproj, mode="tn", tm=rh, tn=1920, tk=2048, out_dtype=F32,
                                name="mm_dwin_sibling", out_blocked=N_CHIPS, rider=_pair_gather_rider(halves))
    reduced = dict(zip(EARLY_REDUCE, shards))
    exchanging, started = _split_start(_pair_exchange_rider([g_sibling], halved=False), "grads_w_in_pair_start")
    g_own = _matmul(h_own, dproj, mode="tn", tm=rh, tn=1920, tk=2048, out_dtype=F32, name="mm_dwin_own",
                    out_blocked=N_CHIPS, after=[started])
    (from_sibling,), _ = _split_wait(exchanging, after=[g_own], name="grads_w_in_pair_wait")
    to_send, own = _add_own_half(g_own, from_sibling, pos_arr, "grads_pair_add_w_in")
    in_flight, started = _split_start(_chip_exchange_rider([to_send], [own]), "grads_w_in_exchange_start")
    grad_x, _, g["norm1_w"] = _matmul_nt_rmsnorm_bwd(dproj, wts["w_in"], x, wts["norm1_w"], dx1, "mm_dh_rms1_bwd",
                                                     after=[started])
    return loss, grad_x, g, reduced, (in_flight, started)


def _mesh_pos():
    return lax.axis_index("x"), lax.axis_index("y"), lax.axis_index("c")


def _other_chips(x, y):
    return [(1 - x, y), (x, 1 - y), (1 - x, 1 - y)]


NEAR_CHIPS, FAR_CHIPS = (0, 1), (2,)


def _cast_into_slot(shard, chip_arr, dtype, name, n_slots=N_CHIPS, after=()):
    r, c = shard.shape
    tr = r // 2 if r % 32 == 0 else r

    def body(chip_ref, s_ref, *refs):
        refs[-1][...] = s_ref[...].astype(dtype)

    return pl.pallas_call(
        body, name=name,
        grid_spec=pltpu.PrefetchScalarGridSpec(
            num_scalar_prefetch=1, grid=(r // tr,),
            in_specs=[pl.BlockSpec((tr, c), lambda i, chip_ref: (i, 0))] + [ANY] * len(after),
            out_specs=pl.BlockSpec((None, tr, c), lambda i, chip_ref: (chip_ref[0], i, 0))),
        out_shape=jax.ShapeDtypeStruct((n_slots, r, c), dtype), compiler_params=_params(),
    )(chip_arr, shard, *after)


GATHER_CHUNKS = 4


def _gather_both_legs_rider(big, small, peers):
    nb = len(big)
    n = nb + len(small)
    nch = GATHER_CHUNKS

    def part(bufs, a, slot, half, ch):
        if a >= nb:
            return bufs[a].at[slot]
        rows = bufs[a].shape[2] // nch
        return bufs[a].at[slot, half, pl.ds(ch * rows, rows)]

    def pieces():
        return [(a, ch, k) for ch in range(nch) for a in range(n) for k in peers if a < nb or ch == 0]

    def ici(bufs, sems, a, ch, k, slot_of_src):
        x, y, c = _mesh_pos()
        px, py = _other_chips(x, y)[k]
        slot = 2 * x + y if slot_of_src == "mine" else 2 * px + py
        return pltpu.make_async_remote_copy(
            src_ref=part(bufs, a, slot, c, ch), dst_ref=part(bufs, a, slot, c, ch), send_sem=sems[0].at[a, ch, k],
            recv_sem=sems[1].at[a, ch, k], device_id=(px, py, c), device_id_type=MESH)

    def forward(bufs, sems, a, ch, k, half):
        x, y, c = _mesh_pos()
        px, py = _other_chips(x, y)[k]
        h = c if half == "mine" else 1 - c
        return pltpu.make_async_remote_copy(
            src_ref=part(bufs, a, 2 * px + py, h, ch), dst_ref=part(bufs, a, 2 * px + py, h, ch),
            send_sem=sems[2].at[a, ch, k], recv_sem=sems[3].at[a, ch, k], device_id=(x, y, 1 - c),
            device_id_type=MESH)

    def start(r_in, bufs, sems):
        for a, ch, k in pieces():
            ici(bufs, sems, a, ch, k, "mine").start()

    def middle(r_in, bufs, sems):
        for a, ch, k in pieces():
            ici(bufs, sems, a, ch, k, "theirs").wait_recv()
            if a < nb:
                forward(bufs, sems, a, ch, k, "mine").start()
        for a, ch, k in pieces():
            ici(bufs, sems, a, ch, k, "mine").wait_send()

    def wait(r_in, bufs, sems):
        for a, ch, k in pieces():
            if a < nb:
                forward(bufs, sems, a, ch, k, "theirs").wait_recv()
        for a, ch, k in pieces():
            if a < nb:
                forward(bufs, sems, a, ch, k, "mine").wait_send()

    ops = list(big) + list(small)
    return _Rider(ops, [jax.ShapeDtypeStruct(o.shape, o.dtype) for o in ops], {i: i for i in range(n)},
                  [pltpu.SemaphoreType.DMA((n, nch, 3)), pltpu.SemaphoreType.DMA((n, nch, 3)),
                   pltpu.SemaphoreType.DMA((nb, nch, 3)), pltpu.SemaphoreType.DMA((nb, nch, 3))],
                  start, wait, middle)


def _comm_call(rider, name):
    def body():
        pass

    return _pallas(body, name=name, grid=(1,), in_specs=[], out_specs=[], out_shape=[], operands=[],
                   rider=rider)[1]


def _gather_ici_rider(big, small):
    nb = len(big)
    n = nb + len(small)

    def copies(bufs, sems):
        x, y, c = _mesh_pos()
        me = 2 * x + y
        part = lambda a, slot: bufs[a].at[slot, c] if a < nb else bufs[a].at[slot]
        out = []
        for a in range(n):
            for k, (px, py) in enumerate(_other_chips(x, y)):
                send = functools.partial(
                    pltpu.make_async_remote_copy,
                    src_ref=part(a, me), dst_ref=part(a, me), send_sem=sems[0].at[a, k],
                    recv_sem=sems[1].at[a, k], device_id=(px, py, c), device_id_type=MESH)
                recv = functools.partial(
                    pltpu.make_async_remote_copy,
                    src_ref=part(a, 2 * px + py), dst_ref=part(a, 2 * px + py), send_sem=sems[0].at[a, k],
                    recv_sem=sems[1].at[a, k], device_id=(px, py, c), device_id_type=MESH)
                out.append((send, recv))
        return out

    def start(r_in, r_out, sems):
        for send, _ in copies(r_out, sems):
            send().start()

    def wait(r_in, r_out, sems):
        cps = copies(r_out, sems)
        for _, recv in cps:
            recv().wait_recv()
        for send, _ in cps:
            send().wait_send()

    ops = list(big) + list(small)
    return _Rider(ops, [jax.ShapeDtypeStruct(o.shape, o.dtype) for o in ops], {i: i for i in range(n)},
                  [pltpu.SemaphoreType.DMA((n, 3)), pltpu.SemaphoreType.DMA((n, 3))], start, wait)


def _gather_forward_rider(big):
    n = len(big)

    def copies(bufs, sems):
        x, y, c = _mesh_pos()
        out = []
        for a in range(n):
            for k, (px, py) in enumerate(_other_chips(x, y)):
                slot = 2 * px + py
                send = functools.partial(
                    pltpu.make_async_remote_copy,
                    src_ref=bufs[a].at[slot, c], dst_ref=bufs[a].at[slot, c], send_sem=sems[0].at[a, k],
                    recv_sem=sems[1].at[a, k], device_id=(x, y, 1 - c), device_id_type=MESH)
                recv = functools.partial(
                    pltpu.make_async_remote_copy,
                    src_ref=bufs[a].at[slot, 1 - c], dst_ref=bufs[a].at[slot, 1 - c], send_sem=sems[0].at[a, k],
                    recv_sem=sems[1].at[a, k], device_id=(x, y, 1 - c), device_id_type=MESH)
                out.append((send, recv))
        return out

    def start(r_in, r_out, sems):
        for send, _ in copies(r_out, sems):
            send().start()

    def wait(r_in, r_out, sems):
        cps = copies(r_out, sems)
        for _, recv in cps:
            recv().wait_recv()
        for send, _ in cps:
            send().wait_send()

    return _Rider(big, [jax.ShapeDtypeStruct(o.shape, o.dtype) for o in big], {i: i for i in range(n)},
                  [pltpu.SemaphoreType.DMA((n, 3)), pltpu.SemaphoreType.DMA((n, 3))], start, wait)


def _pair_exchange_rider(gs, halved):
    n = len(gs)

    def copies(r_in, r_out, sems):
        x, y, c = _mesh_pos()
        return [pltpu.make_async_remote_copy(
            src_ref=r_in[a].at[:, 1 - c] if halved else r_in[a], dst_ref=r_out[a], send_sem=sems[0].at[a],
            recv_sem=sems[1].at[a], device_id=(x, y, 1 - c), device_id_type=MESH) for a in range(n)]

    def start(r_in, r_out, sems):
        for cp in copies(r_in, r_out, sems):
            cp.start()

    def wait(r_in, r_out, sems):
        for cp in copies(r_in, r_out, sems):
            cp.wait()

    return _Rider(gs, [jax.ShapeDtypeStruct((g.shape[0],) + g.shape[-2:], g.dtype) for g in gs], {},
                  [pltpu.SemaphoreType.DMA((n,)), pltpu.SemaphoreType.DMA((n,))], start, wait)


def _chip_exchange_rider(to_send, by_chip, row_range=None):
    n = len(to_send)

    def copies(r_in, r_out, sems):
        x, y, c = _mesh_pos()
        me = 2 * x + y
        rows = (lambda ref: ref) if row_range is None else (lambda ref: ref.at[pl.ds(*row_range)])
        out = []
        for a in range(n):
            for k, (px, py) in enumerate(_other_chips(x, y)):
                send = functools.partial(
                    pltpu.make_async_remote_copy,
                    src_ref=rows(r_in[a].at[2 * px + py]), dst_ref=rows(r_out[a].at[me]),
                    send_sem=sems[0].at[a, k], recv_sem=sems[1].at[a, k], device_id=(px, py, c),
                    device_id_type=MESH)
                recv = functools.partial(
                    pltpu.make_async_remote_copy,
                    src_ref=rows(r_in[a].at[me]), dst_ref=rows(r_out[a].at[2 * px + py]),
                    send_sem=sems[0].at[a, k], recv_sem=sems[1].at[a, k], device_id=(px, py, c),
                    device_id_type=MESH)
                out.append((send, recv))
        return out

    def start(r_in, r_out, sems):
        for send, _ in copies(r_in, r_out, sems):
            send().start()

    def wait(r_in, r_out, sems):
        cps = copies(r_in, r_out, sems)
        for _, recv in cps:
            recv().wait_recv()
        for send, _ in cps:
            send().wait_send()

    return _Rider(list(to_send) + list(by_chip), [jax.ShapeDtypeStruct(b.shape, b.dtype) for b in by_chip],
                  {n + i: i for i in range(n)},
                  [pltpu.SemaphoreType.DMA((n, 3)), pltpu.SemaphoreType.DMA((n, 3))], start, wait)


HBM = pl.BlockSpec(memory_space=pltpu.HBM)
SEM = pl.BlockSpec(memory_space=pltpu.SEMAPHORE)


_IN_FLIGHT = pltpu.CompilerParams(has_side_effects=pltpu.SideEffectType.DATAFLOW_SIDE_EFFECTING)


class _FlatSems:
    def __init__(self, ref, shape):
        self.ref, self.shape = ref, shape

    @property
    def at(self):
        return self

    def __getitem__(self, idx):
        idx = idx if isinstance(idx, tuple) else (idx,)
        flat = 0
        for i, n in zip(idx, self.shape):
            flat = flat * n + i
        return self.ref.at[flat]


def _flat_sem_types(rider):
    return tuple(pltpu.SemaphoreType.DMA((int(np.prod(s.shape)),)) for s in rider.scratch)


def _as_rider_sems(rider, refs):
    return [_FlatSems(r, s.shape) for r, s in zip(refs, rider.scratch)]


def _split_start(rider, name, after=()):
    n_in, n_out, n_sem = len(rider.operands), len(rider.out_shapes), len(rider.scratch)
    n_after = len(after)
    fresh = [j for j in range(n_out) if j not in rider.aliases.values()]
    by_out = {j: i for i, j in rider.aliases.items()}

    def body(*refs):
        r_in = refs[:n_in]
        refs = refs[n_in + n_after:]
        sems = refs[:n_sem]
        thru = refs[n_sem:n_sem + n_in]
        fresh_refs = refs[n_sem + n_in:n_sem + n_in + len(fresh)]
        token = refs[-1]
        r_out = [thru[by_out[j]] if j in by_out else fresh_refs[fresh.index(j)] for j in range(n_out)]
        rider.start(r_in, r_out, _as_rider_sems(rider, sems))
        token[...] = jnp.zeros_like(token)

    res = pl.pallas_call(
        body, name=name,
        out_shape=_flat_sem_types(rider) + tuple(pltpu.HBM(o.shape, o.dtype) for o in rider.operands)
        + tuple(pltpu.HBM(rider.out_shapes[j].shape, rider.out_shapes[j].dtype) for j in fresh)
        + (jax.ShapeDtypeStruct((8, LANES), F32),),
        in_specs=(HBM,) * n_in + (ANY,) * n_after,
        out_specs=(SEM,) * n_sem + (HBM,) * (n_in + len(fresh)) + (pl.BlockSpec(memory_space=pltpu.VMEM),),
        input_output_aliases={i: n_sem + i for i in range(n_in)}, compiler_params=_IN_FLIGHT,
    )(*[pltpu.with_memory_space_constraint(o, pltpu.HBM) for o in rider.operands], *after)
    return (rider, res[:n_sem], res[n_sem:n_sem + n_in], res[n_sem + n_in:-1]), res[-1]


def _split_continue(handles, after, name, phase):
    rider, sems, thru, fresh_arrays = handles
    n_in, n_out, n_sem = len(rider.operands), len(rider.out_shapes), len(rider.scratch)
    fresh = [j for j in range(n_out) if j not in rider.aliases.values()]
    by_out = {j: i for i, j in rider.aliases.items()}
    n_data = n_in + len(fresh)

    def body(*refs):
        r_in = refs[:n_in]
        fresh_refs = refs[n_in:n_data]
        sem_refs = refs[n_data:n_data + n_sem]
        r_out = [r_in[by_out[j]] if j in by_out else fresh_refs[fresh.index(j)] for j in range(n_out)]
        phase(r_in, r_out, _as_rider_sems(rider, sem_refs))

    data = list(thru) + list(fresh_arrays)
    return pl.pallas_call(
        body, name=name, out_shape=tuple(pltpu.HBM(d.shape, d.dtype) for d in data),
        in_specs=(HBM,) * n_data + (SEM,) * n_sem + (ANY,) * len(after), out_specs=(HBM,) * n_data,
        input_output_aliases={i: i for i in range(n_data)}, compiler_params=_IN_FLIGHT,
    )(*data, *sems, *after)


def _split_middle(handles, after, name):
    rider, sems, thru, _ = handles
    res = _split_continue(handles, after, name, rider.middle)
    return rider, sems, res[:len(thru)], res[len(thru):]


def _split_wait(handles, after, name):
    rider = handles[0]
    n_in, n_out = len(rider.operands), len(rider.out_shapes)
    fresh = [j for j in range(n_out) if j not in rider.aliases.values()]
    by_out = {j: i for i, j in rider.aliases.items()}
    res = _split_continue(handles, after, name, rider.wait)
    return [res[by_out[j]] if j in by_out else res[n_in + fresh.index(j)] for j in range(n_out)], res[:n_in]


def _pair_gather_rider(bufs):
    n = len(bufs)

    def copies(r_out, sems):
        x, y, c = _mesh_pos()
        out = []
        for a in range(n):
            send = functools.partial(
                    pltpu.make_async_remote_copy,
                src_ref=r_out[a].at[c], dst_ref=r_out[a].at[c], send_sem=sems[0].at[a],
                recv_sem=sems[1].at[a], device_id=(x, y, 1 - c), device_id_type=MESH)
            recv = functools.partial(
                    pltpu.make_async_remote_copy,
                src_ref=r_out[a].at[1 - c], dst_ref=r_out[a].at[1 - c], send_sem=sems[0].at[a],
                recv_sem=sems[1].at[a], device_id=(x, y, 1 - c), device_id_type=MESH)
            out.append((send, recv))
        return out

    def start(r_in, r_out, sems):
        for send, _ in copies(r_out, sems):
            send().start()

    def wait(r_in, r_out, sems):
        cps = copies(r_out, sems)
        for _, recv in cps:
            recv().wait_recv()
        for send, _ in cps:
            send().wait_send()

    return _Rider(bufs, [jax.ShapeDtypeStruct(b.shape, b.dtype) for b in bufs], {i: i for i in range(n)},
                  [pltpu.SemaphoreType.DMA((n,)), pltpu.SemaphoreType.DMA((n,))], start, wait)


def _add_own_half(g, recv, pos_arr, name):
    nb, rh, cols = g.shape[0], g.shape[-2], g.shape[-1]

    def body(pos_ref, g_ref, r_ref, send_ref, own_ref):
        s = (g_ref[...] + r_ref[...]).astype(BF16)
        send_ref[...] = s

        @pl.when(pl.program_id(0) == pos_ref[1])
        def _():
            own_ref[...] = s

    blk = pl.BlockSpec((None, rh, cols), lambda j, pos_ref: (j, 0, 0))
    g_spec = blk if g.ndim == 3 else pl.BlockSpec((None, None, rh, cols),
                                                   lambda j, pos_ref: (j, pos_ref[0], 0, 0))
    shape = jax.ShapeDtypeStruct((nb, rh, cols), BF16)
    return pl.pallas_call(
        body, name=name,
        grid_spec=pltpu.PrefetchScalarGridSpec(
            num_scalar_prefetch=1, grid=(nb,), in_specs=[g_spec, blk],
            out_specs=[blk, pl.BlockSpec((None, rh, cols), lambda j, pos_ref: (pos_ref[1], 0, 0))]),
        out_shape=[shape, shape], compiler_params=_params(),
    )(pos_arr, g, recv)


def _sum_chips(gath, pos_arr, name):
    nb, rh, cols = gath.shape

    def body(pos_ref, a_ref, b_ref, c_ref, d_ref, o_ref):
        del pos_ref
        o_ref[...] = ((a_ref[...].astype(F32) + b_ref[...].astype(F32)) + c_ref[...].astype(F32)) \
            + d_ref[...].astype(F32)

    tr = rh // 2 if (rh // 2) % 16 == 0 else rh
    specs = [pl.BlockSpec((None, tr, cols), functools.partial(lambda i, pos_ref, j: (j, i, 0), j=j))
             for j in range(nb)]
    return pl.pallas_call(
        body, name=name,
        grid_spec=pltpu.PrefetchScalarGridSpec(
            num_scalar_prefetch=1, grid=(rh // tr,), in_specs=specs,
            out_specs=pl.BlockSpec((None, tr, cols), lambda i, pos_ref: (pos_ref[0], i, 0))),
        out_shape=jax.ShapeDtypeStruct((2, rh, cols), F32), compiler_params=_params(),
    )(pos_arr, gath, gath, gath, gath)


N_DEVICES = 8


def _small_gather_rider(buf):
    def copies(r_out, sems):
        x, y, c = _mesh_pos()
        me = 4 * x + 2 * y + c
        out = []
        for r in range(1, N_DEVICES):
            px = 1 - x if r & 4 else x
            py = 1 - y if r & 2 else y
            pc = 1 - c if r & 1 else c
            out.append(pltpu.make_async_remote_copy(
                src_ref=r_out[0].at[me], dst_ref=r_out[0].at[me], send_sem=sems[0].at[r - 1],
                recv_sem=sems[1].at[r - 1], device_id=(px, py, pc), device_id_type=MESH))
        return out

    def start(r_in, r_out, sems):
        for cp in copies(r_out, sems):
            cp.start()

    def wait(r_in, r_out, sems):
        cps = copies(r_out, sems)
        for cp in cps:
            cp.wait_recv()
        for cp in cps:
            cp.wait_send()

    return _Rider([buf], [jax.ShapeDtypeStruct(buf.shape, buf.dtype)], {0: 0},
                  [pltpu.SemaphoreType.DMA((N_DEVICES - 1,)), pltpu.SemaphoreType.DMA((N_DEVICES - 1,))],
                  start, wait)


def _sum_devices(buf, name):
    def body(b_ref, o_ref):
        acc = b_ref[0]
        for i in range(1, N_DEVICES):
            acc = acc + b_ref[i]
        o_ref[...] = acc

    return _pallas(body, name=name, grid=(1,), in_specs=[pl.BlockSpec(buf.shape, lambda i: (0, 0, 0))],
                   out_specs=pl.BlockSpec(buf.shape[1:], lambda i: (0, 0)),
                   out_shape=jax.ShapeDtypeStruct(buf.shape[1:], F32), operands=[buf])


def _adamw_math(w, g, m, v):
    m = ADAM_B1 * m + (1.0 - ADAM_B1) * g
    v = ADAM_B2 * v + (1.0 - ADAM_B2) * (g * g)
    m_hat = m / (1.0 - ADAM_B1 ** ADAM_STEP)
    v_hat = v / (1.0 - ADAM_B2 ** ADAM_STEP)
    delta = -ADAM_LR * (m_hat / (jnp.sqrt(v_hat) + ADAM_EPS) + ADAM_WD * w)
    return delta, m, v


def _adamw(w, g, m, v, name, after=()):
    r, c = w.shape
    tr = next(t for t in (256, 352, 128, 64) if r % t == 0 and r >= 2 * t)

    def body(w_ref, g_ref, m_ref, v_ref, go_ref, d_ref, mo_ref, vo_ref):
        gv = g_ref[...]
        d, mn, vn = _adamw_math(w_ref[...], gv, m_ref[...], v_ref[...])
        go_ref[...] = gv
        d_ref[...] = d
        mo_ref[...] = mn
        vo_ref[...] = vn

    blk = pl.BlockSpec((tr, c), lambda i: (i, 0))
    return _pallas(body, name=name, grid=(r // tr,), in_specs=[blk] * 4, out_specs=[blk] * 4,
                   out_shape=[jax.ShapeDtypeStruct((r, c), F32)] * 4, operands=[w, g, m, v], after=after)


def _adamw_small(ws, gs, ms, vs, name):
    n = len(ws)

    def body(*refs):
        w_r, g_r, m_r, v_r = refs[:n], refs[n:2 * n], refs[2 * n:3 * n], refs[3 * n:4 * n]
        d_o, m_o, v_o = refs[4 * n:5 * n], refs[5 * n:6 * n], refs[6 * n:7 * n]
        for i in range(n):
            d, mn, vn = _adamw_math(w_r[i][...], g_r[i][...], m_r[i][...], v_r[i][...])
            d_o[i][...] = d
            m_o[i][...] = mn
            v_o[i][...] = vn

    specs = [pl.BlockSpec(w.shape, lambda i: (0, 0)) for w in ws]
    shapes = [jax.ShapeDtypeStruct(w.shape, F32) for w in ws]
    outs = pl.pallas_call(
        body, name=name, grid=(1,), in_specs=specs * 4, out_specs=specs * 3, out_shape=shapes * 3,
        compiler_params=_params(),
    )(*ws, *gs, *ms, *vs)
    return outs[:n], outs[n:2 * n], outs[2 * n:]


BIG = ("w_in", "w_o_attn", "w_pw_conv", "w_out", "w_ffn_in", "w_ffn_out")
ROW_SHARDED = ("w_out", "w_ffn_out")
SMALL = ("norm1_w", "b_gate", "q_norm_w", "k_norm_w", "conv_w", "conv_b", "conv_ln_w", "conv_ln_b", "norm2_w")
ORDER = ("norm1_w", "w_in", "b_gate", "q_norm_w", "k_norm_w", "w_o_attn", "conv_w", "conv_b", "conv_ln_w",
         "conv_ln_b", "w_pw_conv", "w_out", "norm2_w", "w_ffn_in", "w_ffn_out")
PACK_TILE = 8 * LANES


def _pack_small(parts):
    rows = []
    for p in parts:
        flat = p.reshape(-1)
        pad = (-flat.shape[0]) % PACK_TILE
        rows.append(jnp.pad(flat, (0, pad)).reshape(-1, LANES))
    return jnp.concatenate(rows, axis=0)


def _unpack_small(packed, shapes):
    out, row = [], 0
    for shp in shapes:
        size = int(np.prod(shp))
        nrow = -(-size // PACK_TILE) * (PACK_TILE // LANES)
        out.append(packed[row:row + nrow].reshape(-1)[:size].reshape(shp))
        row += nrow
    return out


def kernel(x, positions, norm1_w, w_in, b_gate, q_norm_w, k_norm_w, w_o_attn, conv_w, conv_b, conv_ln_w, conv_ln_b, w_pw_conv, w_out, norm2_w, w_ffn_in, w_ffn_out, loss_target, m_norm1_w, m_w_in, m_b_gate, m_q_norm_w, m_k_norm_w, m_w_o_attn, m_conv_w, m_conv_b, m_conv_ln_w, m_conv_ln_b, m_w_pw_conv, m_w_out, m_norm2_w, m_w_ffn_in, m_w_ffn_out, v_norm1_w, v_w_in, v_b_gate, v_q_norm_w, v_k_norm_w, v_w_o_attn, v_conv_w, v_conv_b, v_conv_ln_w, v_conv_ln_b, v_w_pw_conv, v_w_out, v_norm2_w, v_w_ffn_in, v_w_ffn_out):
    w = dict(norm1_w=norm1_w, w_in=w_in, b_gate=b_gate, q_norm_w=q_norm_w, k_norm_w=k_norm_w, w_o_attn=w_o_attn,
             conv_w=conv_w, conv_b=conv_b, conv_ln_w=conv_ln_w, conv_ln_b=conv_ln_b, w_pw_conv=w_pw_conv,
             w_out=w_out, norm2_w=norm2_w, w_ffn_in=w_ffn_in, w_ffn_out=w_ffn_out)
    m = dict(norm1_w=m_norm1_w, w_in=m_w_in, b_gate=m_b_gate, q_norm_w=m_q_norm_w, k_norm_w=m_k_norm_w,
             w_o_attn=m_w_o_attn, conv_w=m_conv_w, conv_b=m_conv_b, conv_ln_w=m_conv_ln_w,
             conv_ln_b=m_conv_ln_b, w_pw_conv=m_w_pw_conv, w_out=m_w_out, norm2_w=m_norm2_w,
             w_ffn_in=m_w_ffn_in, w_ffn_out=m_w_ffn_out)
    v = dict(norm1_w=v_norm1_w, w_in=v_w_in, b_gate=v_b_gate, q_norm_w=v_q_norm_w, k_norm_w=v_k_norm_w,
             w_o_attn=v_w_o_attn, conv_w=v_conv_w, conv_b=v_conv_b, conv_ln_w=v_conv_ln_w,
             conv_ln_b=v_conv_ln_b, w_pw_conv=v_w_pw_conv, w_out=v_w_out, norm2_w=v_norm2_w,
             w_ffn_in=v_w_ffn_in, w_ffn_out=v_w_ffn_out)
    cx, cy, cc = _mesh_pos()
    chip = 2 * cx + cy

    chip_arr = chip.reshape(1).astype(jnp.int32)
    pos_arr = jnp.stack([cc, chip]).astype(jnp.int32)
    halves = lambda buf: buf.reshape(N_CHIPS, 2, buf.shape[1] // 2, buf.shape[2])
    w_in_buf = halves(_cast_into_slot(w["w_in"][0], chip_arr, BF16, "cast_w_in"))
    small_bufs = [_cast_into_slot(w[n][0], chip_arr, F32, f"slot_{n}") for n in ("conv_w", "b_gate")]
    first_gather, started = _split_start(_gather_both_legs_rider([w_in_buf], small_bufs, NEAR_CHIPS),
                                         "allgather_w_in_near_start")
    late_bufs = [halves(_cast_into_slot(w[n][0], chip_arr, BF16, f"cast_{n}", after=[started]))
                 for n in LATE_GATHER]
    wts = dict(norm1_w=norm1_w, q_norm_w=q_norm_w, k_norm_w=k_norm_w, conv_b=conv_b, conv_ln_w=conv_ln_w,
               conv_ln_b=conv_ln_b, norm2_w=norm2_w)

    loss, grad_x, g, reduced, w_in_in_flight = _forward_backward(
        x[0], positions.reshape(-1, 1), loss_target[0], wts, first_gather, late_bufs, pos_arr)
    grads = {n: b.reshape(-1, b.shape[2]) for n, b in reduced.items()}

    w_in_in_flight, started = w_in_in_flight
    delta, new_m, new_v = {}, {}, {}
    for n in EARLY_REDUCE:
        grads[n], delta[n], new_m[n], new_v[n] = _adamw(w[n][0], grads[n], m[n][0], v[n][0], f"adamw_{n}",
                                                        after=[started])
    small_parts = [loss] + [g[n] for n in SMALL]
    small_shapes = [p.shape for p in small_parts]
    device_arr = (4 * cx + 2 * cy + cc).reshape(1).astype(jnp.int32)
    small_buf = _cast_into_slot(_pack_small(small_parts), device_arr, F32, "slot_small", n_slots=N_DEVICES)

    (by_chip_w_in,), _ = _split_wait(w_in_in_flight, after=[delta[n] for n in EARLY_REDUCE] + [small_buf],
                                     name="grads_w_in_exchange_wait")
    half_w_in = _sum_chips(by_chip_w_in, pos_arr, "grads_chip_sum_w_in")
    shard_w_in, small_buf = _comm_call(
        _riders_together(_pair_gather_rider([half_w_in]), _small_gather_rider(small_buf)),
        "grads_pair_gather_w_in_small_gather")
    summed = _sum_devices(small_buf, "small_sum")
    reduced = _unpack_small(summed, small_shapes)
    loss_total = reduced[0].reshape(())
    for n, r in zip(SMALL, reduced[1:]):
        grads[n] = r
    ch_shard = conv_w.shape[2]
    grads["conv_w"] = lax.dynamic_slice_in_dim(grads["conv_w"], chip * ch_shard, ch_shard, axis=1)
    d_shard = b_gate.shape[2]
    grads["b_gate"] = lax.dynamic_slice_in_dim(grads["b_gate"], chip * d_shard, d_shard, axis=1)

    grads["w_in"], delta["w_in"], new_m["w_in"], new_v["w_in"] = _adamw(
        w["w_in"][0], shard_w_in.reshape(-1, shard_w_in.shape[2]), m["w_in"][0], v["w_in"][0], "adamw_w_in")
    flat2 = lambda a: a.reshape(-1, a.shape[-1])
    d_s, m_s, v_s = _adamw_small([flat2(w[n]) for n in SMALL], [flat2(grads[n]) for n in SMALL],
                                 [flat2(m[n]) for n in SMALL], [flat2(v[n]) for n in SMALL], "adamw_small")
    for i, n in enumerate(SMALL):
        delta[n], new_m[n], new_v[n] = d_s[i], m_s[i], v_s[i]

    shaped = lambda d, n: d[n].reshape(w[n].shape)
    return (loss_total, grad_x[None], *[shaped(grads, n) for n in ORDER], *[shaped(delta, n) for n in ORDER],
            *[shaped(new_m, n) for n in ORDER], *[shaped(new_v, n) for n in ORDER])
```

```python
import functools

import numpy as np
import jax
import jax.numpy as jnp
from jax import lax
from jax.experimental import pallas as pl
from jax.experimental.pallas import tpu as pltpu

F32 = jnp.float32
BF16 = jnp.bfloat16
MESH = pl.DeviceIdType.MESH
ANY = pl.BlockSpec(memory_space=pl.ANY)

HEAD_DIM = 64
N_SLOT_HEADS = 8
DILATIONS = (1, 4, 16)
HALF_SPAN = 64
ROPE_THETA = 500000.0
ROT_DIM = 16
CONV_WIDTH = 31
EPS = 1e-6
NEG_INF = -1e30
ADAM_LR, ADAM_B1, ADAM_B2, ADAM_EPS, ADAM_WD, ADAM_STEP = 0.001, 0.9, 0.999, 1e-08, 0.01, 10

LANES = 128
QBLK = 128
KWIN = QBLK + 2 * HALF_SPAN
VMEM_LIMIT = 52 * 1024 * 1024
N_CHIPS = 4


def _params(**kw):
    return pltpu.CompilerParams(vmem_limit_bytes=VMEM_LIMIT, **kw)


class _Rider:
    def __init__(self, operands, out_shapes, aliases, scratch, start, wait, middle=None):
        self.operands, self.out_shapes, self.aliases = list(operands), list(out_shapes), dict(aliases)
        self.scratch, self.start, self.wait = list(scratch), start, wait
        self.middle = middle


def _riders_together(a, b):
    n_in, n_out, n_sc = len(a.operands), len(a.out_shapes), len(a.scratch)
    aliases = dict(a.aliases)
    aliases.update({n_in + src: n_out + dst for src, dst in b.aliases.items()})

    def start(r_in, r_out, r_sc):
        a.start(r_in[:n_in], r_out[:n_out], r_sc[:n_sc])
        b.start(r_in[n_in:], r_out[n_out:], r_sc[n_sc:])

    def wait(r_in, r_out, r_sc):
        a.wait(r_in[:n_in], r_out[:n_out], r_sc[:n_sc])
        b.wait(r_in[n_in:], r_out[n_out:], r_sc[n_sc:])

    return _Rider(a.operands + b.operands, a.out_shapes + b.out_shapes, aliases, a.scratch + b.scratch, start, wait)


def _pallas(body, *, name, grid, in_specs, out_specs, out_shape, operands, scratch_shapes=(), aliases=None,
            rider=None, after=()):
    single = not isinstance(out_specs, (list, tuple))
    out_specs_l = [out_specs] if single else list(out_specs)
    out_shape_l = [out_shape] if single else list(out_shape)
    aliases = dict(aliases or {})

    def call(fn, all_in_specs, all_out_specs, all_out_shape, all_scratch, all_aliases, all_operands):
        return pl.pallas_call(
            fn, name=name, grid=grid, in_specs=all_in_specs, out_specs=all_out_specs, out_shape=all_out_shape,
            scratch_shapes=all_scratch, input_output_aliases=all_aliases, compiler_params=_params(),
        )(*all_operands)

    if rider is None:
        n_main = len(in_specs)

        def ordered(*refs):
            body(*refs[:n_main], *refs[n_main + len(after):])

        res = call(ordered if after else body, list(in_specs) + [ANY] * len(after), out_specs_l, out_shape_l,
                   list(scratch_shapes), aliases, list(operands) + list(after))
        return res[0] if single else res
    assert not after
    n_in, n_rin = len(in_specs), len(rider.operands)
    n_out, n_rout = len(out_specs_l), len(rider.out_shapes)
    n_sc = len(scratch_shapes)

    def wrapped(*refs):
        main_in, r_in = refs[:n_in], refs[n_in:n_in + n_rin]
        o0 = n_in + n_rin
        main_out, r_out = refs[o0:o0 + n_out], refs[o0 + n_out:o0 + n_out + n_rout]
        s0 = o0 + n_out + n_rout
        main_sc, r_sc = refs[s0:s0 + n_sc], refs[s0 + n_sc:]
        ids = [pl.program_id(d) for d in range(len(grid))]
        first = functools.reduce(jnp.logical_and, [i == 0 for i in ids])
        last = functools.reduce(jnp.logical_and, [i == n - 1 for i, n in zip(ids, grid)])

        @pl.when(first)
        def _():
            rider.start(r_in, r_out, r_sc)

        body(*main_in, *main_out, *main_sc)

        @pl.when(last)
        def _():
            rider.wait(r_in, r_out, r_sc)

    for src, dst in rider.aliases.items():
        aliases[n_in + src] = n_out + dst
    res = call(wrapped, list(in_specs) + [ANY] * n_rin, out_specs_l + [ANY] * n_rout,
               out_shape_l + rider.out_shapes, list(scratch_shapes) + rider.scratch, aliases,
               list(operands) + rider.operands)
    main = res[:n_out]
    return (main[0] if single else main), res[n_out:]


def _matmul(a, b, *, mode, tm, tn, tk, out_dtype, name, b_blocked=False,
            out_blocked=None, cols_outer=False, rider=None, after=()):
    a_shape = a.shape
    if mode == "nn":
        m_dim, k_dim = a_shape
        n_dim = b.shape[0] * b.shape[2] if b_blocked else b.shape[1]
        rows, cols, red = m_dim, n_dim, k_dim
    elif mode == "nt":
        m_dim, n_dim = a_shape
        k_dim = b.shape[1] if b_blocked else b.shape[0]
        rows, cols, red = m_dim, k_dim, n_dim
    else:
        m_dim, k_dim = a_shape
        n_dim = b.shape[1]
        rows, cols, red = k_dim, n_dim, m_dim
    assert rows % tm == 0 and cols % tn == 0 and red % tk == 0, (name, rows, cols, red)
    ni, nj, nk = rows // tm, cols // tn, red // tk

    if mode == "nn":
        a_spec = pl.BlockSpec((tm, tk), lambda i, j, k: (i, k))
        if b_blocked:
            per = b.shape[2] // tn
            b_spec = pl.BlockSpec((None, tk, tn), lambda i, j, k: (j // per, k, j % per))
        else:
            b_spec = pl.BlockSpec((tk, tn), lambda i, j, k: (k, j))
        dims = (((1,), (0,)), ((), ()))
    elif mode == "nt":
        a_spec = pl.BlockSpec((tm, tk), lambda i, j, k: (i, k))
        if b_blocked:
            per = b.shape[2] // tk
            b_spec = pl.BlockSpec((None, tn, tk), lambda i, j, k: (k // per, j, k % per))
        else:
            b_spec = pl.BlockSpec((tn, tk), lambda i, j, k: (j, k))
        dims = (((1,), (1,)), ((), ()))
    else:
        a_spec = pl.BlockSpec((tk, tm), lambda i, j, k: (k, i))
        b_spec = pl.BlockSpec((tk, tn), lambda i, j, k: (k, j))
        dims = (((0,), (0,)), ((), ()))

    if out_blocked:
        per_o = (cols // out_blocked) // tn
        out_spec = pl.BlockSpec((None, tm, tn), lambda i, j, k: (j // per_o, i, j % per_o))
        out_shape = jax.ShapeDtypeStruct((out_blocked, rows, cols // out_blocked), out_dtype)
    else:
        out_spec = pl.BlockSpec((tm, tn), lambda i, j, k: (i, j))
        out_shape = jax.ShapeDtypeStruct((rows, cols), out_dtype)

    def body(a_ref, b_ref, o_ref, *acc):
        prod = lax.dot_general(a_ref[...], b_ref[...], dims, preferred_element_type=F32)
        if nk == 1:
            o_ref[...] = prod.astype(out_dtype)
        else:
            acc_ref, = acc
            k = pl.program_id(2)

            @pl.when(k == 0)
            def _():
                acc_ref[...] = prod

            @pl.when(k > 0)
            def _():
                acc_ref[...] += prod

            @pl.when(k == nk - 1)
            def _():
                o_ref[...] = acc_ref[...].astype(out_dtype)

    scratch = [pltpu.VMEM((tm, tn), F32)] if nk > 1 else []
    grid = (ni, nj, nk)
    if cols_outer:
        swap = lambda spec: pl.BlockSpec(spec.block_shape, lambda j, i, k, f=spec.index_map: f(i, j, k))
        a_spec, b_spec, out_spec, grid = swap(a_spec), swap(b_spec), swap(out_spec), (nj, ni, nk)
    return _pallas(body, name=name, grid=grid, in_specs=[a_spec, b_spec], out_specs=out_spec,
                   out_shape=out_shape, operands=[a, b], scratch_shapes=scratch, rider=rider, after=after)


def _proj_by_slot(h, w_in, order, first, count, name, proj=None, after=()):
    s, k = h.shape
    nb = w_in.shape[2]
    tm = 512
    prev = [] if proj is None else [proj]

    def body(order_ref, h_ref, w_ref, *refs):
        refs[-1][...] = jnp.dot(h_ref[...], w_ref[...], preferred_element_type=F32)

    return pl.pallas_call(
        body, name=name,
        grid_spec=pltpu.PrefetchScalarGridSpec(
            num_scalar_prefetch=1, grid=(count, s // tm),
            in_specs=[pl.BlockSpec((tm, k), lambda j, i, order_ref: (i, 0)),
                      pl.BlockSpec((None, k, nb), lambda j, i, order_ref: (order_ref[first + j], 0, 0))]
            + [ANY] * (len(prev) + len(after)),
            out_specs=pl.BlockSpec((tm, nb), lambda j, i, order_ref: (i, order_ref[first + j]))),
        out_shape=jax.ShapeDtypeStruct((s, N_CHIPS * nb), F32),
        input_output_aliases={3: 0} if prev else {}, compiler_params=_params(),
    )(order, h, w_in, *prev, *after)


def _rmsnorm_fwd(x, w, name):
    s, d = x.shape
    tm = 256

    def body(x_ref, w_ref, o_ref):
        xv = x_ref[...]
        rstd = lax.rsqrt(jnp.mean(xv * xv, axis=-1, keepdims=True) + EPS)
        o_ref[...] = (xv * rstd * w_ref[...]).astype(BF16)

    return pl.pallas_call(
        body, name=name, grid=(s // tm,),
        in_specs=[pl.BlockSpec((tm, d), lambda i: (i, 0)), pl.BlockSpec((1, d), lambda i: (0, 0))],
        out_specs=pl.BlockSpec((tm, d), lambda i: (i, 0)),
        out_shape=jax.ShapeDtypeStruct((s, d), BF16), compiler_params=_params(),
    )(x, w)


def _matmul_nt_rmsnorm_bwd(dz, w_blocked, x, w, dres, name, rider=None, after=()):
    s, d = x.shape
    nk, _, nb = w_blocked.shape
    tm = 512
    nt_dims = (((1,), (1,)), ((), ()))

    def body(a_ref, b_ref, x_ref, w_ref, dres_ref, dx_ref, dxb_ref, dw_ref, acc_ref):
        k, i = pl.program_id(0), pl.program_id(1)
        rows = pl.ds(pl.multiple_of(i * tm, tm), tm)
        prod = lax.dot_general(a_ref[...], b_ref[...], nt_dims, preferred_element_type=F32)

        @pl.when(k == 0)
        def _():
            acc_ref[rows, :] = prod

        @pl.when(jnp.logical_and(k > 0, k < nk - 1))
        def _():
            acc_ref[rows, :] += prod

        @pl.when(k == nk - 1)
        def _():
            xv = x_ref[...]
            rstd = lax.rsqrt(jnp.mean(xv * xv, axis=-1, keepdims=True) + EPS)
            xhat = xv * rstd
            dhv = acc_ref[rows, :] + prod
            g = dhv * w_ref[...]
            dx = rstd * (g - xhat * jnp.mean(g * xhat, axis=-1, keepdims=True)) + dres_ref[...]
            dx_ref[...] = dx
            dxb_ref[...] = dx.astype(BF16)
            part = jnp.sum(dhv * xhat, axis=0, keepdims=True)

            @pl.when(i == 0)
            def _():
                dw_ref[...] = part

            @pl.when(i > 0)
            def _():
                dw_ref[...] += part

    assert nk >= 2
    row = pl.BlockSpec((tm, d), lambda k, i: (jnp.where(k == nk - 1, i, 0), 0))
    vec = pl.BlockSpec((1, d), lambda k, i: (0, 0))
    return _pallas(
        body, name=name, grid=(nk, s // tm),
        in_specs=[pl.BlockSpec((tm, nb), lambda k, i: (i, k)), pl.BlockSpec((None, d, nb), lambda k, i: (k, 0, 0)),
                  row, vec, row],
        out_specs=[row, row, vec],
        out_shape=[jax.ShapeDtypeStruct((s, d), F32), jax.ShapeDtypeStruct((s, d), BF16),
                   jax.ShapeDtypeStruct((1, d), F32)],
        operands=[dz, w_blocked, x, w, dres], scratch_shapes=[pltpu.VMEM((s, d), F32)], rider=rider, after=after)


def _rope_consts():
    lane = np.arange(LANES)
    in_head = lane % HEAD_DIM
    inv_freq = ROPE_THETA ** (-jnp.arange(0, ROT_DIM, 2, dtype=F32) / ROT_DIM)
    invf = jnp.where(jnp.asarray(in_head < ROT_DIM), jnp.tile(inv_freq, LANES // (ROT_DIM // 2)), 0.0)
    m_a = np.where(in_head < ROT_DIM // 2, -1.0, 0.0).astype(np.float32)
    m_b = np.where((in_head >= ROT_DIM // 2) & (in_head < ROT_DIM), 1.0, 0.0).astype(np.float32)
    block_diag = (lane[:, None] // HEAD_DIM == lane[None, :] // HEAD_DIM).astype(np.float32)
    return (invf.reshape(1, LANES).astype(F32), jnp.asarray(m_a).reshape(1, LANES),
            jnp.asarray(m_b).reshape(1, LANES), jnp.asarray(block_diag, dtype=BF16))


def _head_sums(v, bd):
    hi = v.astype(BF16)
    lo = (v - hi.astype(F32)).astype(BF16)
    return jnp.dot(hi, bd, preferred_element_type=F32) + jnp.dot(lo, bd, preferred_element_type=F32)


def _rope_tables(pos_col, consts, name, after=()):
    s = pos_col.shape[0]
    tm = 512
    invf, m_a, m_b, _ = consts

    def body(pos_ref, invf_ref, ma_ref, mb_ref, cos_ref, sa_ref, sb_ref):
        ang = pos_ref[...].astype(F32) * invf_ref[...]
        sin = jnp.sin(ang)
        cos_ref[...] = jnp.cos(ang)
        sa_ref[...] = sin * ma_ref[...]
        sb_ref[...] = sin * mb_ref[...]

    vec = pl.BlockSpec((1, LANES), lambda i: (0, 0))
    tab = pl.BlockSpec((tm, LANES), lambda i: (i, 0))
    return _pallas(body, name=name, grid=(s // tm,), in_specs=[pl.BlockSpec((tm, 1), lambda i: (i, 0)), vec, vec, vec],
                   out_specs=[tab] * 3, out_shape=[jax.ShapeDtypeStruct((s, LANES), F32)] * 3,
                   operands=[pos_col, invf, m_a, m_b], after=after)


def _qk_fwd(proj, rope, qw2, kw2, bd, name, rider=None):
    s = proj.shape[0]
    width = 3 * N_SLOT_HEADS * HEAD_DIM
    tm = 128
    scale = HEAD_DIM ** -0.5

    def body(q_ref, k_ref, cos_ref, sa_ref, sb_ref, qw_ref, kw_ref, bd_ref, qo_ref, ko_ref):
        cos, s_a, s_b = cos_ref[...], sa_ref[...], sb_ref[...]
        bdv = bd_ref[...]
        for src, w_ref, dst, sc in ((q_ref, qw_ref, qo_ref, scale), (k_ref, kw_ref, ko_ref, 1.0)):
            for cb in range(width // LANES):
                cols = slice(cb * LANES, (cb + 1) * LANES)
                t = src[:, cols]
                rstd = lax.rsqrt(_head_sums(t * t, bdv) * (1.0 / HEAD_DIM) + EPS)
                y = t * rstd * w_ref[...]
                r = y * cos + pltpu.roll(y, LANES - 8, axis=1) * s_a + pltpu.roll(y, 8, axis=1) * s_b
                dst[:, cols] = r * sc if sc != 1.0 else r

    vec = pl.BlockSpec((1, LANES), lambda i: (0, 0))
    tab = pl.BlockSpec((tm, LANES), lambda i: (i, 0))
    return _pallas(
        body, name=name, grid=(s // tm,),
        in_specs=[pl.BlockSpec((tm, width), lambda i: (i, 0)), pl.BlockSpec((tm, width), lambda i: (i, 1)),
                  tab, tab, tab, vec, vec, pl.BlockSpec((LANES, LANES), lambda i: (0, 0))],
        out_specs=[pl.BlockSpec((tm, width), lambda i: (i, 0))] * 2,
        out_shape=[jax.ShapeDtypeStruct((s, width), F32)] * 2,
        operands=[proj, proj, *rope, qw2, kw2, bd], rider=rider)


def _qk_bwd(dqn, dkn, dv, da, db, dgl, proj, rope, qw2, kw2, bd, name, rider=None):
    s = proj.shape[0]
    width = 3 * N_SLOT_HEADS * HEAD_DIM
    ch = da.shape[1]
    gate_w = dgl.shape[1]
    out_w = 3 * width + 2 * ch + gate_w
    assert out_w == proj.shape[1]
    tm = 128
    scale = HEAD_DIM ** -0.5

    def body(dq_ref, dk_ref, dv_ref, da_ref, db_ref, dgl_ref, q_ref, k_ref, cos_ref, sa_ref, sb_ref, qw_ref, kw_ref,
             bd_ref, out_ref, dqw_ref, dkw_ref):
        cos, s_a, s_b = cos_ref[...], sa_ref[...], sb_ref[...]
        bdv = bd_ref[...]
        first = pl.program_id(0) == 0
        for src, dsrc, w_ref, col0, dw_ref, sc in ((q_ref, dq_ref, qw_ref, 0, dqw_ref, scale),
                                                   (k_ref, dk_ref, kw_ref, width, dkw_ref, 1.0)):
            dw_acc = jnp.zeros((1, LANES), F32)
            for cb in range(width // LANES):
                cols = slice(cb * LANES, (cb + 1) * LANES)
                t = src[:, cols]
                dr = dsrc[:, cols]
                if sc != 1.0:
                    dr = dr * sc
                dy = dr * cos + pltpu.roll(dr * s_a, 8, axis=1) + pltpu.roll(dr * s_b, LANES - 8, axis=1)
                rstd = lax.rsqrt(_head_sums(t * t, bdv) * (1.0 / HEAD_DIM) + EPS)
                xhat = t * rstd
                g = dy * w_ref[...]
                dt = rstd * (g - xhat * (_head_sums(g * xhat, bdv) * (1.0 / HEAD_DIM)))
                out_ref[:, col0 + cb * LANES: col0 + (cb + 1) * LANES] = dt.astype(BF16)
                dw_acc = dw_acc + jnp.sum(dy * xhat, axis=0, keepdims=True)
            dw_acc = dw_acc + pltpu.roll(dw_acc, HEAD_DIM, axis=1)

            @pl.when(first)
            def _(dw_ref=dw_ref, dw_acc=dw_acc):
                dw_ref[...] = dw_acc

            @pl.when(jnp.logical_not(first))
            def _(dw_ref=dw_ref, dw_acc=dw_acc):
                dw_ref[...] += dw_acc
        out_ref[:, 2 * width: 3 * width] = dv_ref[...].astype(BF16)
        out_ref[:, 3 * width: 3 * width + ch] = da_ref[...]
        out_ref[:, 3 * width + ch: 3 * width + 2 * ch] = db_ref[...]
        out_ref[:, 3 * width + 2 * ch: out_w] = dgl_ref[...]

    vec = pl.BlockSpec((1, LANES), lambda i: (0, 0))
    blk = lambda c: pl.BlockSpec((tm, width), lambda i: (i, c))
    cblk = pl.BlockSpec((tm, ch), lambda i: (i, 0))
    tab = pl.BlockSpec((tm, LANES), lambda i: (i, 0))
    return _pallas(
        body, name=name, grid=(s // tm,),
        in_specs=[blk(0), blk(0), blk(0), cblk, cblk, pl.BlockSpec((tm, gate_w), lambda i: (i, 0)),
                  blk(0), blk(1), tab, tab, tab, vec, vec, pl.BlockSpec((LANES, LANES), lambda i: (0, 0))],
        out_specs=[pl.BlockSpec((tm, out_w), lambda i: (i, 0)), vec, vec],
        out_shape=[jax.ShapeDtypeStruct((s, out_w), BF16)] + [jax.ShapeDtypeStruct((1, LANES), F32)] * 2,
        operands=[dqn, dkn, dv, da, db, dgl, proj, proj, *rope, qw2, kw2, bd], rider=rider)


def _row_chunks(n_rows, fn, chunk=256):
    def step(i, c):
        fn(pl.ds(pl.multiple_of(i * chunk, chunk), chunk))
        return c
    lax.fori_loop(0, n_rows // chunk, step, 0)


def _to_residue_major(dst, src, s, d, dst_off=0, cast=None):
    seq = s // d
    for r in range(d):
        v = src[...] if d == 1 else src[pl.ds(r, seq, stride=d), :]
        dst[dst_off + r * seq: dst_off + (r + 1) * seq, :] = v if cast is None else v.astype(cast)


def _from_residue_major(dst, src, s, d, src_off=0):
    seq = s // d
    for r in range(d):
        v = src[src_off + r * seq: src_off + (r + 1) * seq, :]
        if d == 1:
            dst[...] = v
        else:
            dst[pl.ds(r, seq, stride=d), :] = v


def _band_bias():
    qi = lax.broadcasted_iota(jnp.int32, (QBLK, KWIN), 0)
    kj = lax.broadcasted_iota(jnp.int32, (QBLK, KWIN), 1)
    return jnp.where(jnp.abs(kj - HALF_SPAN - qi) <= HALF_SPAN, 0.0, NEG_INF).astype(F32)


def _range_bias(base, seq):
    kj = lax.broadcasted_iota(jnp.int32, (1, KWIN), 1)
    lo = (base & -seq) - base + HALF_SPAN
    return jnp.where((kj >= lo) & (kj < lo + seq), 0.0, NEG_INF).astype(F32)


def _skewed_blocks(n_blk, produce, consume):
    produce(0, 0)
    for b in range(n_blk):
        consume(b, b % 2)
        if b + 1 < n_blk:
            produce(b + 1, (b + 1) % 2)


def _block_base(b):
    return b * QBLK if isinstance(b, int) else pl.multiple_of(b * QBLK, QBLK)


def _attn_fwd(qn, kn, proj, name, rider=None):
    s = qn.shape[0]
    n_pairs = N_SLOT_HEADS * HEAD_DIM // LANES
    v_col0 = 2 * qn.shape[1] // LANES
    nt_dims = (((1,), (1,)), ((), ()))

    def body(q_ref, k_ref, v_ref, attn_ref, lse_ref, attn_b_ref, q_rm, k_rm, v_rm, acc_rm, m_rm, l_rm,
             acc_p, m_p, l_p, m_run, l_run, acc_run, band, s_buf, m_buf):
        g = pl.program_id(1)
        zpad = jnp.zeros((HALF_SPAN, LANES), BF16)
        k_rm[0:HALF_SPAN, :] = zpad
        k_rm[s + HALF_SPAN: s + 2 * HALF_SPAN, :] = zpad
        v_rm[0:HALF_SPAN, 0:LANES] = zpad
        v_rm[s + HALF_SPAN: s + 2 * HALF_SPAN, 0:LANES] = zpad

        def ones_rows(rows):
            v_rm[pl.ds(rows.start, rows.size), LANES:2 * LANES] = jnp.ones((rows.size, LANES), BF16)

        _row_chunks(s + 2 * HALF_SPAN, ones_rows, chunk=2 * HALF_SPAN)
        band[...] = _band_bias()
        lane = lax.broadcasted_iota(jnp.int32, (QBLK, LANES), 1)
        low = lane < HEAD_DIM
        n_blk = s // QBLK

        for gi, d in enumerate(DILATIONS):
            @pl.when(g == gi)
            def _(gi=gi, d=d):
                seq = s // d
                _to_residue_major(q_rm, q_ref, s, d, cast=BF16)
                _to_residue_major(k_rm, k_ref, s, d, dst_off=HALF_SPAN, cast=BF16)
                _to_residue_major(v_rm.at[:, 0:LANES], v_ref, s, d, dst_off=HALF_SPAN, cast=BF16)

                def scores(b, slot):
                    base = _block_base(b)
                    q = q_rm[pl.ds(base, QBLK), :]
                    zero = jnp.zeros_like(q)
                    q2 = jnp.concatenate([jnp.where(low, q, zero), jnp.where(low, zero, q)], axis=0)
                    sc = lax.dot_general(q2, k_rm[pl.ds(base, KWIN), :], nt_dims, preferred_element_type=F32)
                    bias = band[...] + _range_bias(base, seq)
                    for hh in range(2):
                        rows = slice(hh * QBLK, (hh + 1) * QBLK)
                        sh = sc[rows, :] + bias
                        s_buf[slot, rows, :] = sh
                        m_buf[slot, rows, :] = jnp.broadcast_to(jnp.max(sh, axis=-1, keepdims=True), (QBLK, LANES))

                def outputs(b, slot):
                    base = _block_base(b)
                    sv = s_buf[slot]
                    mb = m_buf[slot]
                    p = jnp.exp(jnp.concatenate([sv[:, 0:LANES] - mb, sv[:, LANES:2 * LANES] - mb], axis=1))
                    pv = jnp.dot(p.astype(BF16), v_rm[pl.ds(base, KWIN), :], preferred_element_type=F32)
                    rows = pl.ds(base, QBLK)
                    acc_rm[rows, :] = jnp.where(low, pv[0:QBLK, 0:LANES], pv[QBLK:2 * QBLK, 0:LANES])
                    l_rm[rows, :] = jnp.where(low, pv[0:QBLK, LANES:2 * LANES], pv[QBLK:2 * QBLK, LANES:2 * LANES])
                    m_rm[rows, :] = jnp.where(low, mb[0:QBLK, :], mb[QBLK:2 * QBLK, :])

                _skewed_blocks(n_blk, scores, outputs)
                if d == 1:
                    src = (acc_rm, m_rm, l_rm)
                else:
                    for dst_, src_ in ((acc_p, acc_rm), (m_p, m_rm), (l_p, l_rm)):
                        _from_residue_major(dst_, src_, s, d)
                    src = (acc_p, m_p, l_p)

                def combine(rows):
                    a_g, m_g, l_g = src[0][rows, :], src[1][rows, :], src[2][rows, :]
                    if gi == 0:
                        m_new, l_new, a_new = m_g, l_g, a_g
                    else:
                        m_old = m_run[rows, :]
                        m_new = jnp.maximum(m_old, m_g)
                        w_old = jnp.exp(m_old - m_new)
                        w_g = jnp.exp(m_g - m_new)
                        l_new = l_run[rows, :] * w_old + l_g * w_g
                        a_new = acc_run[rows, :] * w_old + a_g * w_g
                    if gi == len(DILATIONS) - 1:
                        out = a_new / l_new
                        attn_ref[rows, :] = out
                        attn_b_ref[rows, :] = out.astype(BF16)
                        lse_ref[rows, :] = m_new + jnp.log(l_new)
                    else:
                        m_run[rows, :] = m_new
                        l_run[rows, :] = l_new
                        acc_run[rows, :] = a_new

                _row_chunks(s, combine)

    qk_spec = pl.BlockSpec((s, LANES), lambda hp, g: (0, g * n_pairs + hp))
    v_spec = pl.BlockSpec((s, LANES), lambda hp, g: (0, v_col0 + g * n_pairs + hp))
    o_spec = pl.BlockSpec((s, LANES), lambda hp, g: (0, hp))
    f32buf = pltpu.VMEM((s, LANES), F32)
    return _pallas(
        body, name=name, grid=(n_pairs, len(DILATIONS)), in_specs=[qk_spec, qk_spec, v_spec],
        out_specs=[o_spec, o_spec, o_spec],
        out_shape=[jax.ShapeDtypeStruct((s, n_pairs * LANES), F32)] * 2
        + [jax.ShapeDtypeStruct((s, n_pairs * LANES), BF16)],
        operands=[qn, kn, proj],
        scratch_shapes=[pltpu.VMEM((s, LANES), BF16), pltpu.VMEM((s + 2 * HALF_SPAN, LANES), BF16),
                        pltpu.VMEM((s + 2 * HALF_SPAN, 2 * LANES), BF16)] + [f32buf] * 9
        + [pltpu.VMEM((QBLK, KWIN), F32), pltpu.VMEM((2, 2 * QBLK, KWIN), F32),
           pltpu.VMEM((2, 2 * QBLK, LANES), F32)],
        rider=rider)


def _attn_bwd(qn, kn, proj, dattn, attn, lse, bd, name, rider=None):
    s = qn.shape[0]
    n_pairs = N_SLOT_HEADS * HEAD_DIM // LANES
    v_col0 = 2 * qn.shape[1] // LANES
    nt_dims = (((1,), (1,)), ((), ()))
    tn_dims = (((0,), (0,)), ((), ()))
    spad = s + 2 * HALF_SPAN

    def body(q_ref, k_ref, v_ref, do_ref, o_ref, lse_ref, bd_ref, dq_ref, dk_ref, dv_ref,
             q_rm, k_rm, v_rm, do_rm, lse0_rm, lse1_rm, dd0_rm, dd1_rm, dq_rm, dk_rm, dv_rm,
             lse0_p, lse1_p, dd0_p, dd1_p, band, p_buf, ds_buf):
        g = pl.program_id(1)
        zpad = jnp.zeros((HALF_SPAN, LANES), BF16)
        for buf in (k_rm, v_rm):
            buf[0:HALF_SPAN, :] = zpad
            buf[s + HALF_SPAN: spad, :] = zpad
        zf = jnp.zeros((HALF_SPAN, LANES), F32)
        for buf in (dk_rm, dv_rm):
            buf[0:HALF_SPAN, :] = zf
            buf[s + HALF_SPAN: spad, :] = zf
        band[...] = _band_bias()

        def clear(rows):
            z = jnp.zeros((rows.size, LANES), F32)
            dk_rm[pl.ds(rows.start + HALF_SPAN, rows.size), :] = z
            dv_rm[pl.ds(rows.start + HALF_SPAN, rows.size), :] = z

        _row_chunks(s, clear)

        def prepare(rows):
            lo = lax.broadcasted_iota(jnp.int32, (rows.size, LANES), 1) < HEAD_DIM
            dsum = _head_sums(do_ref[rows, :] * o_ref[rows, :], bd_ref[...])
            dswap = pltpu.roll(dsum, HEAD_DIM, axis=1)
            dd0_p[rows, :] = jnp.where(lo, dsum, dswap)
            dd1_p[rows, :] = jnp.where(lo, dswap, dsum)
            lv = lse_ref[rows, :]
            lswap = pltpu.roll(lv, HEAD_DIM, axis=1)
            lse0_p[rows, :] = jnp.where(lo, lv, lswap)
            lse1_p[rows, :] = jnp.where(lo, lswap, lv)

        @pl.when(g == 0)
        def _():
            _row_chunks(s, prepare)
        lane = lax.broadcasted_iota(jnp.int32, (QBLK, LANES), 1)
        low = lane < HEAD_DIM
        n_blk = s // QBLK

        def stacked(ref, rows):
            val = ref[rows, :]
            zero = jnp.zeros_like(val)
            return jnp.concatenate([jnp.where(low, val, zero), jnp.where(low, zero, val)], axis=0)

        for gi, d in enumerate(DILATIONS):
            @pl.when(g == gi)
            def _(d=d):
                seq = s // d
                _to_residue_major(q_rm, q_ref, s, d, cast=BF16)
                _to_residue_major(k_rm, k_ref, s, d, dst_off=HALF_SPAN, cast=BF16)
                _to_residue_major(v_rm, v_ref, s, d, dst_off=HALF_SPAN, cast=BF16)
                _to_residue_major(do_rm, do_ref, s, d, cast=BF16)
                for dst_, src_ in ((lse0_rm, lse0_p), (lse1_rm, lse1_p), (dd0_rm, dd0_p), (dd1_rm, dd1_p)):
                    _to_residue_major(dst_, src_, s, d)

                def scores(b, slot):
                    base = _block_base(b)
                    rows = pl.ds(base, QBLK)
                    win = pl.ds(base, KWIN)
                    sc = lax.dot_general(stacked(q_rm, rows), k_rm[win, :], nt_dims, preferred_element_type=F32)
                    dp = lax.dot_general(stacked(do_rm, rows), v_rm[win, :], nt_dims, preferred_element_type=F32)
                    bias = band[...] + _range_bias(base, seq)
                    for hh, (lse_r, dd_r) in enumerate(((lse0_rm, dd0_rm), (lse1_rm, dd1_rm))):
                        r = slice(hh * QBLK, (hh + 1) * QBLK)
                        lse_h = lse_r[rows, :]
                        dd_h = dd_r[rows, :]
                        sh = sc[r, :] + bias
                        p = jnp.exp(jnp.concatenate([sh[:, 0:LANES] - lse_h, sh[:, LANES:KWIN] - lse_h], axis=1))
                        dph = dp[r, :]
                        ds = p * jnp.concatenate([dph[:, 0:LANES] - dd_h, dph[:, LANES:KWIN] - dd_h], axis=1)
                        p_buf[slot, r, :] = p.astype(BF16)
                        ds_buf[slot, r, :] = ds.astype(BF16)

                def grads(b, slot):
                    base = _block_base(b)
                    rows = pl.ds(base, QBLK)
                    win = pl.ds(base, KWIN)
                    p = p_buf[slot]
                    ds = ds_buf[slot]
                    dq2 = jnp.dot(ds, k_rm[win, :], preferred_element_type=F32)
                    dq_rm[rows, :] = jnp.where(low, dq2[0:QBLK, :], dq2[QBLK:2 * QBLK, :])
                    dk_rm[win, :] += lax.dot_general(ds, stacked(q_rm, rows), tn_dims, preferred_element_type=F32)
                    dv_rm[win, :] += lax.dot_general(p, stacked(do_rm, rows), tn_dims, preferred_element_type=F32)

                _skewed_blocks(n_blk, scores, grads)
                _from_residue_major(dq_ref, dq_rm, s, d)
                _from_residue_major(dk_ref, dk_rm, s, d, src_off=HALF_SPAN)
                _from_residue_major(dv_ref, dv_rm, s, d, src_off=HALF_SPAN)

    qk_spec = pl.BlockSpec((s, LANES), lambda hp, g: (0, g * n_pairs + hp))
    v_spec = pl.BlockSpec((s, LANES), lambda hp, g: (0, v_col0 + g * n_pairs + hp))
    o_spec = pl.BlockSpec((s, LANES), lambda hp, g: (0, hp))
    width = qn.shape[1]
    f32buf = pltpu.VMEM((s, LANES), F32)
    f32pad = pltpu.VMEM((spad, LANES), F32)
    return _pallas(
        body, name=name, grid=(n_pairs, len(DILATIONS)),
        in_specs=[qk_spec, qk_spec, v_spec, o_spec, o_spec, o_spec,
                  pl.BlockSpec((LANES, LANES), lambda hp, g: (0, 0))],
        out_specs=[qk_spec, qk_spec, qk_spec],
        out_shape=[jax.ShapeDtypeStruct((s, width), F32)] * 3,
        operands=[qn, kn, proj, dattn, attn, lse, bd],
        scratch_shapes=[pltpu.VMEM((s, LANES), BF16), pltpu.VMEM((spad, LANES), BF16),
                        pltpu.VMEM((spad, LANES), BF16), pltpu.VMEM((s, LANES), BF16),
                        f32buf, f32buf, f32buf, f32buf, f32buf, f32pad, f32pad,
                        f32buf, f32buf, f32buf, f32buf, pltpu.VMEM((QBLK, KWIN), F32),
                        pltpu.VMEM((2, 2 * QBLK, KWIN), BF16), pltpu.VMEM((2, 2 * QBLK, KWIN), BF16)],
        rider=rider)


CONV_PAD = 16


def _conv_fwd(proj, conv_w, conv_b, col0, name, rider=None):
    s = proj.shape[0]
    ch = conv_w.shape[1]
    nblk = ch // LANES
    a0 = col0 // LANES
    tr = 256
    shift = CONV_PAD - (CONV_WIDTH - 1) // 2

    def body(a_ref, b_ref, w_ref, bias_ref, u0_ref, uc_ref, pad):
        z = jnp.zeros((CONV_PAD, LANES), F32)
        pad[0:CONV_PAD, :] = z
        pad[s + CONV_PAD: s + 2 * CONV_PAD, :] = z

        def glu(rows):
            u0 = a_ref[rows, :] * jax.nn.sigmoid(b_ref[rows, :])
            u0_ref[rows, :] = u0
            pad[pl.ds(rows.start + CONV_PAD, rows.size), :] = u0

        _row_chunks(s, glu)
        for t in range(0, s, tr):
            acc = jnp.broadcast_to(bias_ref[...], (tr, LANES))
            for k in range(CONV_WIDTH):
                acc = acc + w_ref[k:k + 1, :] * pad[t + k + shift: t + k + shift + tr, :]
            uc_ref[t:t + tr, :] = acc

    return _pallas(
        body, name=name, grid=(nblk,),
        in_specs=[pl.BlockSpec((s, LANES), lambda c: (0, a0 + c)),
                  pl.BlockSpec((s, LANES), lambda c: (0, a0 + nblk + c)),
                  pl.BlockSpec((CONV_WIDTH, LANES), lambda c: (0, c)),
                  pl.BlockSpec((1, LANES), lambda c: (0, c))],
        out_specs=[pl.BlockSpec((s, LANES), lambda c: (0, c))] * 2,
        out_shape=[jax.ShapeDtypeStruct((s, ch), F32)] * 2, operands=[proj, proj, conv_w, conv_b],
        scratch_shapes=[pltpu.VMEM((s + 2 * CONV_PAD, LANES), F32)], rider=rider)


def _ln_silu_fwd(uc, ln_w, ln_b, name):
    s, ch = uc.shape
    tm = 256

    def body(u_ref, w_ref, b_ref, o_ref):
        u = u_ref[...]
        mu = jnp.mean(u, axis=-1, keepdims=True)
        xc = u - mu
        rstd = lax.rsqrt(jnp.mean(xc * xc, axis=-1, keepdims=True) + EPS)
        z = xc * rstd * w_ref[...] + b_ref[...]
        o_ref[...] = (z * jax.nn.sigmoid(z)).astype(BF16)

    row = pl.BlockSpec((tm, ch), lambda i: (i, 0))
    vec = pl.BlockSpec((1, ch), lambda i: (0, 0))
    return pl.pallas_call(
        body, name=name, grid=(s // tm,), in_specs=[row, vec, vec], out_specs=row,
        out_shape=jax.ShapeDtypeStruct((s, ch), BF16), compiler_params=_params(),
    )(uc, ln_w, ln_b)


def _ln_silu_bwd(du3, uc, ln_w, ln_b, name):
    s, ch = uc.shape
    tm = 256

    def body(d_ref, u_ref, w_ref, b_ref, du_ref, dw_ref, db_ref):
        u = u_ref[...]
        mu = jnp.mean(u, axis=-1, keepdims=True)
        xc = u - mu
        rstd = lax.rsqrt(jnp.mean(xc * xc, axis=-1, keepdims=True) + EPS)
        xhat = xc * rstd
        z = xhat * w_ref[...] + b_ref[...]
        sg = jax.nn.sigmoid(z)
        dz = d_ref[...] * (sg * (1.0 + z * (1.0 - sg)))
        dxh = dz * w_ref[...]
        du_ref[...] = rstd * (dxh - jnp.mean(dxh, axis=-1, keepdims=True)
                              - xhat * jnp.mean(dxh * xhat, axis=-1, keepdims=True))
        pw = jnp.sum(dz * xhat, axis=0, keepdims=True)
        pb = jnp.sum(dz, axis=0, keepdims=True)
        first = pl.program_id(0) == 0

        @pl.when(first)
        def _():
            dw_ref[...] = pw
            db_ref[...] = pb

        @pl.when(jnp.logical_not(first))
        def _():
            dw_ref[...] += pw
            db_ref[...] += pb

    row = pl.BlockSpec((tm, ch), lambda i: (i, 0))
    vec = pl.BlockSpec((1, ch), lambda i: (0, 0))
    return pl.pallas_call(
        body, name=name, grid=(s // tm,), in_specs=[row, row, vec, vec], out_specs=[row, vec, vec],
        out_shape=[jax.ShapeDtypeStruct((s, ch), F32), jax.ShapeDtypeStruct((1, ch), F32),
                   jax.ShapeDtypeStruct((1, ch), F32)],
        compiler_params=_params(),
    )(du3, uc, ln_w, ln_b)


def _conv_bwd(duc, u0, proj, conv_w, col0, name, rider=None):
    s = proj.shape[0]
    ch = conv_w.shape[1]
    nblk = ch // LANES
    a0 = col0 // LANES
    tr = 256
    half = (CONV_WIDTH - 1) // 2
    shift = CONV_PAD - half

    def body(duc_ref, u0_ref, a_ref, b_ref, w_ref, da_ref, db_ref, dw_ref, dbias_ref, pad_d, pad_u):
        z = jnp.zeros((CONV_PAD, LANES), F32)
        for buf in (pad_d, pad_u):
            buf[0:CONV_PAD, :] = z
            buf[s + CONV_PAD: s + 2 * CONV_PAD, :] = z

        def fill(rows):
            dst = pl.ds(rows.start + CONV_PAD, rows.size)
            pad_d[dst, :] = duc_ref[rows, :]
            pad_u[dst, :] = u0_ref[rows, :]

        _row_chunks(s, fill)
        dw_acc = [jnp.zeros((8, LANES), F32) for _ in range(CONV_WIDTH)]
        dbias_acc = jnp.zeros((8, LANES), F32)
        for t in range(0, s, tr):
            d_t = duc_ref[t:t + tr, :]
            dbias_acc = dbias_acc + jnp.sum(d_t.reshape(tr // 8, 8, LANES), axis=0)
            du0 = jnp.zeros((tr, LANES), F32)
            for k in range(CONV_WIDTH):
                du0 = du0 + w_ref[k:k + 1, :] * pad_d[t - k + half + CONV_PAD: t - k + half + CONV_PAD + tr, :]
                prod = d_t * pad_u[t + k + shift: t + k + shift + tr, :]
                dw_acc[k] = dw_acc[k] + jnp.sum(prod.reshape(tr // 8, 8, LANES), axis=0)
            av = a_ref[t:t + tr, :]
            sg = jax.nn.sigmoid(b_ref[t:t + tr, :])
            da_ref[t:t + tr, :] = (du0 * sg).astype(BF16)
            db_ref[t:t + tr, :] = (du0 * av * sg * (1.0 - sg)).astype(BF16)
        for k in range(CONV_WIDTH):
            dw_ref[k:k + 1, :] = jnp.sum(dw_acc[k], axis=0, keepdims=True)
        dbias_ref[...] = jnp.sum(dbias_acc, axis=0, keepdims=True)

    col = lambda off: pl.BlockSpec((s, LANES), lambda c: (0, off + c))
    return _pallas(
        body, name=name, grid=(nblk,),
        in_specs=[col(0), col(0), col(a0), col(a0 + nblk),
                  pl.BlockSpec((CONV_WIDTH, LANES), lambda c: (0, c))],
        out_specs=[col(0), col(0), pl.BlockSpec((CONV_WIDTH, LANES), lambda c: (0, c)),
                   pl.BlockSpec((1, LANES), lambda c: (0, c))],
        out_shape=[jax.ShapeDtypeStruct((s, ch), BF16)] * 2
        + [jax.ShapeDtypeStruct((CONV_WIDTH, ch), F32), jax.ShapeDtypeStruct((1, ch), F32)],
        operands=[duc, u0, proj, proj, conv_w],
        scratch_shapes=[pltpu.VMEM((s + 2 * CONV_PAD, LANES), F32)] * 2, rider=rider)


def _mix_out_proj(attn_b, u3, w_o, w_pw, proj, bg, col0, w_out, x, norm_w, name, rider=None):
    s, d = x.shape
    k = attn_b.shape[1]
    tm = 256
    half = d // 2
    assert col0 % half == 0
    c0 = col0 // half

    def body(a_ref, u_ref, wo_ref, wp_ref, a0_ref, a1_ref, b0_ref, b1_ref, bias_ref, w_ref, x_ref, nw_ref,
             ya_ref, yb_ref, mixed_ref, x1_ref, h2_ref):
        mixed = None
        for br, (src_ref, wb_ref, lo_ref, hi_ref, y_ref) in enumerate(((a_ref, wo_ref, a0_ref, a1_ref, ya_ref),
                                                                       (u_ref, wp_ref, b0_ref, b1_ref, yb_ref))):
            src = src_ref[...]
            y = jnp.concatenate([jnp.dot(src, wb_ref[j], preferred_element_type=F32) for j in range(N_CHIPS)],
                                axis=1)
            y_ref[...] = y
            logits = jnp.concatenate([lo_ref[...], hi_ref[...]], axis=1)
            part = jax.nn.sigmoid(logits + bias_ref[br]) * y
            mixed = part if mixed is None else mixed + part
        mixed = mixed.astype(BF16)
        mixed_ref[...] = mixed
        x1 = x_ref[...] + jnp.dot(mixed, w_ref[...], preferred_element_type=F32)
        x1_ref[...] = x1
        rstd = lax.rsqrt(jnp.mean(x1 * x1, axis=-1, keepdims=True) + EPS)
        h2_ref[...] = (x1 * rstd * nw_ref[...]).astype(BF16)

    row = pl.BlockSpec((tm, d), lambda i: (i, 0))
    src_row = pl.BlockSpec((tm, k), lambda i: (i, 0))
    blocks = pl.BlockSpec(w_o.shape, lambda i: (0, 0, 0))
    logit_blk = lambda j: pl.BlockSpec((tm, half), functools.partial(lambda i, j: (i, c0 + j), j=j))
    return _pallas(
        body, name=name, grid=(s // tm,),
        in_specs=[src_row, src_row, blocks, blocks, logit_blk(0), logit_blk(1), logit_blk(2), logit_blk(3),
                  pl.BlockSpec((2, 1, d), lambda i: (0, 0, 0)), pl.BlockSpec((d, d), lambda i: (0, 0)), row,
                  pl.BlockSpec((1, d), lambda i: (0, 0))],
        out_specs=[row] * 5,
        out_shape=[jax.ShapeDtypeStruct((s, d), F32), jax.ShapeDtypeStruct((s, d), F32),
                   jax.ShapeDtypeStruct((s, d), BF16), jax.ShapeDtypeStruct((s, d), F32),
                   jax.ShapeDtypeStruct((s, d), BF16)],
        operands=[attn_b, u3, w_o, w_pw, proj, proj, proj, proj, bg, w_out, x, norm_w], rider=rider)


def _out_proj_bwd_gates(dx1, w_out, proj, bg, y_a, y_b, w_o, w_pw, col0, name, after=()):
    s, d = y_a.shape
    n_blk, k, blk = w_o.shape
    tm = 256
    half = d // 2
    assert col0 % half == 0
    c0 = col0 // half
    nt_dims = (((1,), (1,)), ((), ()))

    def body(dx_ref, w_ref, a0_ref, a1_ref, b0_ref, b1_ref, bias_ref, ya_ref, yb_ref, wo_ref, wp_ref,
             dgl_ref, dya_ref, dyb_ref, db_ref, da_ref, du_ref):
        dm = lax.dot_general(dx_ref[...], w_ref[...], nt_dims, preferred_element_type=F32)
        parts = []
        for br, (lo_ref, hi_ref, y_ref, dy_ref, wb_ref, dsrc_ref) in enumerate((
                (a0_ref, a1_ref, ya_ref, dya_ref, wo_ref, da_ref), (b0_ref, b1_ref, yb_ref, dyb_ref, wp_ref, du_ref))):
            logits = jnp.concatenate([lo_ref[...], hi_ref[...]], axis=1)
            gate = jax.nn.sigmoid(logits + bias_ref[br])
            dy = (dm * gate).astype(BF16)
            dy_ref[...] = dy
            dsrc = lax.dot_general(dy[:, 0:blk], wb_ref[0], nt_dims, preferred_element_type=F32)
            for j in range(1, n_blk):
                dsrc = dsrc + lax.dot_general(dy[:, j * blk:(j + 1) * blk], wb_ref[j], nt_dims,
                                              preferred_element_type=F32)
            dsrc_ref[...] = dsrc
            dgl = dm * y_ref[...] * gate * (1.0 - gate)
            dgl_ref[:, br * d:(br + 1) * d] = dgl.astype(BF16)
            parts.append(jnp.sum(dgl, axis=0, keepdims=True))
        part = jnp.concatenate(parts, axis=0)
        first = pl.program_id(0) == 0

        @pl.when(first)
        def _():
            db_ref[...] = part

        @pl.when(jnp.logical_not(first))
        def _():
            db_ref[...] += part

    row = pl.BlockSpec((tm, d), lambda i: (i, 0))
    logit_blk = lambda k: pl.BlockSpec((tm, half), functools.partial(lambda i, k: (i, c0 + k), k=k))
    blocks = pl.BlockSpec(w_o.shape, lambda i: (0, 0, 0))
    src_row = pl.BlockSpec((tm, k), lambda i: (i, 0))
    return _pallas(
        body, name=name, grid=(s // tm,),
        in_specs=[row, pl.BlockSpec((d, d), lambda i: (0, 0)), logit_blk(0), logit_blk(1), logit_blk(2),
                  logit_blk(3), pl.BlockSpec((2, 1, d), lambda i: (0, 0, 0)), row, row, blocks, blocks],
        out_specs=[pl.BlockSpec((tm, 2 * d), lambda i: (i, 0)), row, row, pl.BlockSpec((2, d), lambda i: (0, 0)),
                   src_row, src_row],
        out_shape=[jax.ShapeDtypeStruct((s, 2 * d), BF16), jax.ShapeDtypeStruct((s, d), BF16),
                   jax.ShapeDtypeStruct((s, d), BF16), jax.ShapeDtypeStruct((2, d), F32),
                   jax.ShapeDtypeStruct((s, k), F32), jax.ShapeDtypeStruct((s, k), F32)],
        operands=[dx1, w_out, proj, proj, proj, proj, bg, y_a, y_b, w_o, w_pw], after=after)


def _ffn_in_swiglu(h2, w_blocked, name):
    s, k = h2.shape
    nblk, _, tn = w_blocked.shape
    ff = nblk // 2 * tn
    tm = 512

    def body(a_ref, wg_ref, wu_ref, g_ref, u_ref, act_ref):
        a = a_ref[...]
        gt = jnp.dot(a, wg_ref[...], preferred_element_type=F32)
        up = jnp.dot(a, wu_ref[...], preferred_element_type=F32)
        g_ref[...] = gt
        u_ref[...] = up
        act_ref[...] = (gt * jax.nn.sigmoid(gt) * up).astype(BF16)

    out = pl.BlockSpec((tm, tn), lambda j, i: (i, j))
    return pl.pallas_call(
        body, name=name, grid=(nblk // 2, s // tm),
        in_specs=[pl.BlockSpec((tm, k), lambda j, i: (i, 0)),
                  pl.BlockSpec((None, k, tn), lambda j, i: (j, 0, 0)),
                  pl.BlockSpec((None, k, tn), lambda j, i: (nblk // 2 + j, 0, 0))],
        out_specs=[out, out, out],
        out_shape=[jax.ShapeDtypeStruct((s, ff), F32), jax.ShapeDtypeStruct((s, ff), F32),
                   jax.ShapeDtypeStruct((s, ff), BF16)],
        compiler_params=_params(),
    )(h2, w_blocked, w_blocked)


def _ffn_out_bwd_swiglu(dy, w_ffn_out, gate, up, name, rider=None):
    s, d = dy.shape
    ff = gate.shape[1]
    tm = 256
    nt_dims = (((1,), (1,)), ((), ()))

    def body(dy_ref, w_ref, g_ref, u_ref, o_ref):
        dv = lax.dot_general(dy_ref[...], w_ref[...], nt_dims, preferred_element_type=F32)
        gt = g_ref[...]
        sg = jax.nn.sigmoid(gt)
        o_ref[:, 0:ff] = (dv * u_ref[...] * (sg * (1.0 + gt * (1.0 - sg)))).astype(BF16)
        o_ref[:, ff:2 * ff] = (dv * gt * sg).astype(BF16)

    row = pl.BlockSpec((tm, ff), lambda i: (i, 0))
    return _pallas(
        body, name=name, grid=(s // tm,),
        in_specs=[pl.BlockSpec((tm, d), lambda i: (i, 0)), pl.BlockSpec((ff, d), lambda i: (0, 0)), row, row],
        out_specs=pl.BlockSpec((tm, 2 * ff), lambda i: (i, 0)),
        out_shape=jax.ShapeDtypeStruct((s, 2 * ff), BF16), operands=[dy, w_ffn_out, gate, up], rider=rider)


def _ffn_out_loss(act, w_ffn_out, x1, target, name):
    s, k = act.shape
    d = w_ffn_out.shape[1]
    tm = 512

    def body(a_ref, w_ref, x1_ref, t_ref, dy_ref, dyb_ref, loss_ref, acc):
        y = x1_ref[...] + jnp.dot(a_ref[...], w_ref[...], preferred_element_type=F32)
        diff = y - t_ref[...]
        dy = diff * (1.0 / d)
        dy_ref[...] = dy
        dyb_ref[...] = dy.astype(BF16)
        part = jnp.sum((diff * diff).reshape(tm // 8, 8, d), axis=0)
        i = pl.program_id(0)

        @pl.when(i == 0)
        def _():
            acc[...] = part

        @pl.when(i > 0)
        def _():
            acc[...] += part

        @pl.when(i == pl.num_programs(0) - 1)
        def _():
            loss_ref[...] = (0.5 / d) * jnp.sum(jnp.sum(acc[...], axis=1, keepdims=True), axis=0, keepdims=True)

    row = pl.BlockSpec((tm, d), lambda i: (i, 0))
    return pl.pallas_call(
        body, name=name, grid=(s // tm,),
        in_specs=[pl.BlockSpec((tm, k), lambda i: (i, 0)), pl.BlockSpec((k, d), lambda i: (0, 0)), row, row],
        out_specs=[row, row, pl.BlockSpec((1, 1), lambda i: (0, 0))],
        out_shape=[jax.ShapeDtypeStruct((s, d), F32), jax.ShapeDtypeStruct((s, d), BF16),
                   jax.ShapeDtypeStruct((1, 1), F32)],
        scratch_shapes=[pltpu.VMEM((8, d), F32)], compiler_params=_params(),
    )(act, w_ffn_out, x1, target)


LATE_GATHER = ("w_o_attn", "w_pw_conv", "w_out", "w_ffn_in", "w_ffn_out")
EARLY_REDUCE = LATE_GATHER


def _blocks_by_half(g):
    if g.ndim == 2:
        g = g.reshape(N_CHIPS, g.shape[0] // N_CHIPS, g.shape[1])
    return g.reshape(N_CHIPS, 2, g.shape[1] // 2, g.shape[2])


def _forward_backward(x, pos_col, target, wts, first_gather, late_bufs, pos_arr):
    wts = dict(wts)
    consts = _rope_consts()
    bd = consts[3]
    qw2 = jnp.tile(wts["q_norm_w"], (1, LANES // HEAD_DIM))
    kw2 = jnp.tile(wts["k_norm_w"], (1, LANES // HEAD_DIM))
    qkv_w = 3 * N_SLOT_HEADS * HEAD_DIM
    conv_col0 = 3 * qkv_w

    h = _rmsnorm_fwd(x, wts["norm1_w"], "rms1_fwd")
    slot_order = jnp.bitwise_xor(pos_arr[1], jnp.asarray([0, 2, 1, 3], jnp.int32))
    blocked = lambda buf: buf.reshape(N_CHIPS, -1, buf.shape[3])
    near = first_gather
    rope = _rope_tables(pos_col, consts, "rope_tables", after=list(late_bufs))
    proj = _proj_by_slot(h, blocked(near[2][0]), slot_order, 0, 1, "mm_proj_own", after=list(rope))
    near = _split_middle(near, after=[proj], name="allgather_w_in_near_forward")
    far, _ = _split_start(_gather_both_legs_rider(near[2][:1], near[2][1:], FAR_CHIPS), "allgather_w_in_far_start")
    _, bufs = _split_wait((near[0], near[1], far[2], near[3]), after=[], name="allgather_w_in_near_wait")
    proj = _proj_by_slot(h, blocked(bufs[0]), slot_order, 1, len(NEAR_CHIPS), "mm_proj_near", proj=proj)
    far = _split_middle((far[0], far[1], bufs, far[3]), after=[proj], name="allgather_w_in_far_forward")
    (w_in_buf, conv_w_buf, b_gate_buf), _ = _split_wait(far, after=[], name="allgather_w_in_far_wait")
    wts["w_in"] = w_in_buf.reshape(N_CHIPS, -1, w_in_buf.shape[3])
    wts["conv_w"] = conv_w_buf.transpose(1, 0, 2).reshape(CONV_WIDTH, -1)
    wts["b_gate"] = b_gate_buf.transpose(1, 0, 2).reshape(2, 1, -1)
    ch = wts["conv_w"].shape[1]
    gate_col0 = conv_col0 + 2 * ch
    n_mix = LATE_GATHER.index("w_ffn_in")
    mix_rider, ffn_rider = _gather_ici_rider(late_bufs[:n_mix], []), _gather_ici_rider(late_bufs[n_mix:], [])
    both, started = _split_start(_riders_together(mix_rider, ffn_rider), "late_gather_start", after=[wts["w_in"]])
    n_sem, n_buf = len(mix_rider.scratch), len(mix_rider.operands)
    mix_gather = (mix_rider, both[1][:n_sem], both[2][:n_buf], [])
    ffn_gather = (ffn_rider, both[1][n_sem:], both[2][n_buf:], [])
    proj = _proj_by_slot(h, wts["w_in"], slot_order, 1 + len(NEAR_CHIPS), len(FAR_CHIPS), "mm_proj_far", proj=proj,
                         after=[started])
    qn, kn = _qk_fwd(proj, rope, qw2, kw2, bd, "qk_fwd")
    attn, lse, attn_b = _attn_fwd(qn, kn, proj, "attn_fwd")

    def gathered(names, bufs):
        for n, buf in zip(names, bufs):
            full = buf.reshape(N_CHIPS, -1, buf.shape[3])
            wts[n] = full.reshape(-1, full.shape[2]) if n in ROW_SHARDED else full

    mix_bufs, _ = _split_wait(mix_gather, after=[attn_b], name="late_gather_mix_wait")
    (u0, uc), mix_bufs = _conv_fwd(proj, wts["conv_w"], wts["conv_b"], conv_col0, "conv_fwd",
                                   rider=_gather_forward_rider(mix_bufs))
    gathered(LATE_GATHER[:n_mix], mix_bufs)
    u3 = _ln_silu_fwd(uc, wts["conv_ln_w"], wts["conv_ln_b"], "ln_fwd")
    ffn_bufs, _ = _split_wait(ffn_gather, after=[u3], name="late_gather_ffn_wait")
    (y_a, y_b, mixed, x1, h2), ffn_bufs = _mix_out_proj(
        attn_b, u3, wts["w_o_attn"], wts["w_pw_conv"], proj, wts["b_gate"], gate_col0, wts["w_out"], x,
        wts["norm2_w"], "mix_x1_rms2", rider=_gather_forward_rider(ffn_bufs))
    gathered(LATE_GATHER[n_mix:], ffn_bufs)
    gate, up, act = _ffn_in_swiglu(h2, wts["w_ffn_in"], "mm_gu_swiglu")
    dy, dy_b16, loss = _ffn_out_loss(act, wts["w_ffn_out"], x1, target, "mm_x2_loss")

    g = {}
    by_chip = {}

    def pair_add(n, blocks, received):
        return _add_own_half(blocks, received, pos_arr, f"grads_pair_add_{n}")

    g_ffn_out = _blocks_by_half(
        _matmul(act, dy_b16, mode="tn", tm=1408, tn=1024, tk=2048, out_dtype=F32, name="mm_dwffnout"))
    dgu, (received,) = _ffn_out_bwd_swiglu(dy_b16, wts["w_ffn_out"], gate, up, "mm_dact_swiglu_bwd",
                                           rider=_pair_exchange_rider([g_ffn_out], halved=True))
    to_send, own = pair_add("w_ffn_out", g_ffn_out, received)
    (dx1, dx1_b16, g["norm2_w"]), (by_chip["w_ffn_out"],) = _matmul_nt_rmsnorm_bwd(
        dgu, wts["w_ffn_in"], x1, wts["norm2_w"], dy, "mm_dh2_rms2_bwd", rider=_chip_exchange_rider([to_send], [own]))
    g_ffn_in = _blocks_by_half(_matmul(h2, dgu, mode="tn", tm=512, tn=1408, tk=2048, out_dtype=F32,
                                       name="mm_dwffnin", out_blocked=N_CHIPS, cols_outer=True))
    exchanging, started = _split_start(_pair_exchange_rider([g_ffn_in], halved=True), "grads_ffn_in_pair_start")
    g["w_out"] = _matmul(mixed, dx1_b16, mode="tn", tm=512, tn=1024, tk=2048, out_dtype=F32, name="mm_dwout",
                         after=[started])
    dgl, dy_a, dy_b, g["b_gate"], dattn, du3 = _out_proj_bwd_gates(
        dx1_b16, wts["w_out"], proj, wts["b_gate"], y_a, y_b, wts["w_o_attn"], wts["w_pw_conv"], gate_col0,
        "mix_bwd")
    (received,), (g_ffn_in,) = _split_wait(exchanging, after=[dgl], name="grads_ffn_in_pair_wait")
    ffn_in_to_send, ffn_in_own = pair_add("w_ffn_in", g_ffn_in, received)
    g["w_o_attn"] = _matmul(attn_b, dy_a, mode="tn", tm=512, tn=256, tk=2048, out_dtype=F32, name="mm_dwo",
                            out_blocked=N_CHIPS)
    g["w_pw_conv"] = _matmul(u3, dy_b, mode="tn", tm=512, tn=256, tk=2048, out_dtype=F32, name="mm_dwpw",
                             out_blocked=N_CHIPS)
    duc, g["conv_ln_w"], g["conv_ln_b"] = _ln_silu_bwd(du3, uc, wts["conv_ln_w"], wts["conv_ln_b"], "ln_bwd")

    small3 = ("w_out", "w_o_attn", "w_pw_conv")
    g_small3 = [_blocks_by_half(g.pop(n)) for n in small3]
    (da, db, g["conv_w"], g["conv_b"]), received = _conv_bwd(
        duc, u0, proj, wts["conv_w"], conv_col0, "conv_bwd", rider=_pair_exchange_rider(g_small3, halved=True))
    sums3 = [pair_add(n, gb, rv) for n, gb, rv in zip(small3, g_small3, received)]
    (dqn, dkn, dv), (by_chip["w_ffn_in"],) = _attn_bwd(
        qn, kn, proj, dattn, attn, lse, bd, "attn_bwd",
        rider=_chip_exchange_rider([ffn_in_to_send], [ffn_in_own]))
    (dproj, dqw, dkw), exchanged3 = _qk_bwd(
        dqn, dkn, dv, da, db, dgl, proj, rope, qw2, kw2, bd, "qk_bwd",
        rider=_chip_exchange_rider([s[0] for s in sums3], [s[1] for s in sums3]))
    by_chip.update(zip(small3, exchanged3))
    halves = [_sum_chips(by_chip[n], pos_arr, f"grads_chip_sum_{n}") for n in EARLY_REDUCE]
    g["q_norm_w"] = dqw[:, :HEAD_DIM]
    g["k_norm_w"] = dkw[:, :HEAD_DIM]

    c = pos_arr[0]
    rh = h.shape[1] // 2
    h_sibling = lax.dynamic_slice_in_dim(h, (1 - c) * rh, rh, axis=1)
    h_own = lax.dynamic_slice_in_dim(h, c * rh, rh, axis=1)
    g_sibling, shards = _matmul(h_sibling, dproj, mode="tn", tm=rh, tn=1920, tk=2048, out_dtype=F32,
                                name="mm_dwin_sibling", out_blocked=N_CHIPS, rider=_pair_gather_rider(halves))
    reduced = dict(zip(EARLY_REDUCE, shards))
    exchanging, started = _split_start(_pair_exchange_rider([g_sibling], halved=False), "grads_w_in_pair_start")
    g_own = _matmul(h_own, dproj, mode="tn", tm=rh, tn=1920, tk=2048, out_dtype=F32, name="mm_dwin_own",
                    out_blocked=N_CHIPS, after=[started])
    (from_sibling,), _ = _split_wait(exchanging, after=[g_own], name="grads_w_in_pair_wait")
    to_send, own = _add_own_half(g_own, from_sibling, pos_arr, "grads_pair_add_w_in")
    in_flight, started = _split_start(_chip_exchange_rider([to_send], [own]), "grads_w_in_exchange_start")
    grad_x, _, g["norm1_w"] = _matmul_nt_rmsnorm_bwd(dproj, wts["w_in"], x, wts["norm1_w"], dx1, "mm_dh_rms1_bwd",
                                                     after=[started])
    return loss, grad_x, g, reduced, (in_flight, started)


def _mesh_pos():
    return lax.axis_index("x"), lax.axis_index("y"), lax.axis_index("c")


def _other_chips(x, y):
    return [(1 - x, y), (x, 1 - y), (1 - x, 1 - y)]


NEAR_CHIPS, FAR_CHIPS = (0, 1), (2,)


def _cast_into_slot(shard, chip_arr, dtype, name, n_slots=N_CHIPS, after=()):
    r, c = shard.shape
    tr = r // 2 if r % 32 == 0 else r

    def body(chip_ref, s_ref, *refs):
        refs[-1][...] = s_ref[...].astype(dtype)

    return pl.pallas_call(
        body, name=name,
        grid_spec=pltpu.PrefetchScalarGridSpec(
            num_scalar_prefetch=1, grid=(r // tr,),
            in_specs=[pl.BlockSpec((tr, c), lambda i, chip_ref: (i, 0))] + [ANY] * len(after),
            out_specs=pl.BlockSpec((None, tr, c), lambda i, chip_ref: (chip_ref[0], i, 0))),
        out_shape=jax.ShapeDtypeStruct((n_slots, r, c), dtype), compiler_params=_params(),
    )(chip_arr, shard, *after)


GATHER_CHUNKS = 4


def _gather_both_legs_rider(big, small, peers):
    nb = len(big)
    n = nb + len(small)
    nch = GATHER_CHUNKS

    def part(bufs, a, slot, half, ch):
        if a >= nb:
            return bufs[a].at[slot]
        rows = bufs[a].shape[2] // nch
        return bufs[a].at[slot, half, pl.ds(ch * rows, rows)]

    def pieces():
        return [(a, ch, k) for ch in range(nch) for a in range(n) for k in peers if a < nb or ch == 0]

    def ici(bufs, sems, a, ch, k, slot_of_src):
        x, y, c = _mesh_pos()
        px, py = _other_chips(x, y)[k]
        slot = 2 * x + y if slot_of_src == "mine" else 2 * px + py
        return pltpu.make_async_remote_copy(
            src_ref=part(bufs, a, slot, c, ch), dst_ref=part(bufs, a, slot, c, ch), send_sem=sems[0].at[a, ch, k],
            recv_sem=sems[1].at[a, ch, k], device_id=(px, py, c), device_id_type=MESH)

    def forward(bufs, sems, a, ch, k, half):
        x, y, c = _mesh_pos()
        px, py = _other_chips(x, y)[k]
        h = c if half == "mine" else 1 - c
        return pltpu.make_async_remote_copy(
            src_ref=part(bufs, a, 2 * px + py, h, ch), dst_ref=part(bufs, a, 2 * px + py, h, ch),
            send_sem=sems[2].at[a, ch, k], recv_sem=sems[3].at[a, ch, k], device_id=(x, y, 1 - c),
            device_id_type=MESH)

    def start(r_in, bufs, sems):
        for a, ch, k in pieces():
            ici(bufs, sems, a, ch, k, "mine").start()

    def middle(r_in, bufs, sems):
        for a, ch, k in pieces():
            ici(bufs, sems, a, ch, k, "theirs").wait_recv()
            if a < nb:
                forward(bufs, sems, a, ch, k, "mine").start()
        for a, ch, k in pieces():
            ici(bufs, sems, a, ch, k, "mine").wait_send()

    def wait(r_in, bufs, sems):
        for a, ch, k in pieces():
            if a < nb:
                forward(bufs, sems, a, ch, k, "theirs").wait_recv()
        for a, ch, k in pieces():
            if a < nb:
                forward(bufs, sems, a, ch, k, "mine").wait_send()

    ops = list(big) + list(small)
    return _Rider(ops, [jax.ShapeDtypeStruct(o.shape, o.dtype) for o in ops], {i: i for i in range(n)},
                  [pltpu.SemaphoreType.DMA((n, nch, 3)), pltpu.SemaphoreType.DMA((n, nch, 3)),
                   pltpu.SemaphoreType.DMA((nb, nch, 3)), pltpu.SemaphoreType.DMA((nb, nch, 3))],
                  start, wait, middle)


def _comm_call(rider, name):
    def body():
        pass

    return _pallas(body, name=name, grid=(1,), in_specs=[], out_specs=[], out_shape=[], operands=[],
                   rider=rider)[1]


def _gather_ici_rider(big, small):
    nb = len(big)
    n = nb + len(small)

    def copies(bufs, sems):
        x, y, c = _mesh_pos()
        me = 2 * x + y
        part = lambda a, slot: bufs[a].at[slot, c] if a < nb else bufs[a].at[slot]
        out = []
        for a in range(n):
            for k, (px, py) in enumerate(_other_chips(x, y)):
                send = functools.partial(
                    pltpu.make_async_remote_copy,
                    src_ref=part(a, me), dst_ref=part(a, me), send_sem=sems[0].at[a, k],
                    recv_sem=sems[1].at[a, k], device_id=(px, py, c), device_id_type=MESH)
                recv = functools.partial(
                    pltpu.make_async_remote_copy,
                    src_ref=part(a, 2 * px + py), dst_ref=part(a, 2 * px + py), send_sem=sems[0].at[a, k],
                    recv_sem=sems[1].at[a, k], device_id=(px, py, c), device_id_type=MESH)
                out.append((send, recv))
        return out

    def start(r_in, r_out, sems):
        for send, _ in copies(r_out, sems):
            send().start()

    def wait(r_in, r_out, sems):
        cps = copies(r_out, sems)
        for _, recv in cps:
            recv().wait_recv()
        for send, _ in cps:
            send().wait_send()

    ops = list(big) + list(small)
    return _Rider(ops, [jax.ShapeDtypeStruct(o.shape, o.dtype) for o in ops], {i: i for i in range(n)},
                  [pltpu.SemaphoreType.DMA((n, 3)), pltpu.SemaphoreType.DMA((n, 3))], start, wait)


def _gather_forward_rider(big):
    n = len(big)

    def copies(bufs, sems):
        x, y, c = _mesh_pos()
        out = []
        for a in range(n):
            for k, (px, py) in enumerate(_other_chips(x, y)):
                slot = 2 * px + py
                send = functools.partial(
                    pltpu.make_async_remote_copy,
                    src_ref=bufs[a].at[slot, c], dst_ref=bufs[a].at[slot, c], send_sem=sems[0].at[a, k],
                    recv_sem=sems[1].at[a, k], device_id=(x, y, 1 - c), device_id_type=MESH)
                recv = functools.partial(
                    pltpu.make_async_remote_copy,
                    src_ref=bufs[a].at[slot, 1 - c], dst_ref=bufs[a].at[slot, 1 - c], send_sem=sems[0].at[a, k],
                    recv_sem=sems[1].at[a, k], device_id=(x, y, 1 - c), device_id_type=MESH)
                out.append((send, recv))
        return out

    def start(r_in, r_out, sems):
        for send, _ in copies(r_out, sems):
            send().start()

    def wait(r_in, r_out, sems):
        cps = copies(r_out, sems)
        for _, recv in cps:
            recv().wait_recv()
        for send, _ in cps:
            send().wait_send()

    return _Rider(big, [jax.ShapeDtypeStruct(o.shape, o.dtype) for o in big], {i: i for i in range(n)},
                  [pltpu.SemaphoreType.DMA((n, 3)), pltpu.SemaphoreType.DMA((n, 3))], start, wait)


def _pair_exchange_rider(gs, halved):
    n = len(gs)

    def copies(r_in, r_out, sems):
        x, y, c = _mesh_pos()
        return [pltpu.make_async_remote_copy(
            src_ref=r_in[a].at[:, 1 - c] if halved else r_in[a], dst_ref=r_out[a], send_sem=sems[0].at[a],
            recv_sem=sems[1].at[a], device_id=(x, y, 1 - c), device_id_type=MESH) for a in range(n)]

    def start(r_in, r_out, sems):
        for cp in copies(r_in, r_out, sems):
            cp.start()

    def wait(r_in, r_out, sems):
        for cp in copies(r_in, r_out, sems):
            cp.wait()

    return _Rider(gs, [jax.ShapeDtypeStruct((g.shape[0],) + g.shape[-2:], g.dtype) for g in gs], {},
                  [pltpu.SemaphoreType.DMA((n,)), pltpu.SemaphoreType.DMA((n,))], start, wait)


def _chip_exchange_rider(to_send, by_chip, row_range=None):
    n = len(to_send)

    def copies(r_in, r_out, sems):
        x, y, c = _mesh_pos()
        me = 2 * x + y
        rows = (lambda ref: ref) if row_range is None else (lambda ref: ref.at[pl.ds(*row_range)])
        out = []
        for a in range(n):
            for k, (px, py) in enumerate(_other_chips(x, y)):
                send = functools.partial(
                    pltpu.make_async_remote_copy,
                    src_ref=rows(r_in[a].at[2 * px + py]), dst_ref=rows(r_out[a].at[me]),
                    send_sem=sems[0].at[a, k], recv_sem=sems[1].at[a, k], device_id=(px, py, c),
                    device_id_type=MESH)
                recv = functools.partial(
                    pltpu.make_async_remote_copy,
                    src_ref=rows(r_in[a].at[me]), dst_ref=rows(r_out[a].at[2 * px + py]),
                    send_sem=sems[0].at[a, k], recv_sem=sems[1].at[a, k], device_id=(px, py, c),
                    device_id_type=MESH)
                out.append((send, recv))
        return out

    def start(r_in, r_out, sems):
        for send, _ in copies(r_in, r_out, sems):
            send().start()

    def wait(r_in, r_out, sems):
        cps = copies(r_in, r_out, sems)
        for _, recv in cps:
            recv().wait_recv()
        for send, _ in cps:
            send().wait_send()

    return _Rider(list(to_send) + list(by_chip), [jax.ShapeDtypeStruct(b.shape, b.dtype) for b in by_chip],
                  {n + i: i for i in range(n)},
                  [pltpu.SemaphoreType.DMA((n, 3)), pltpu.SemaphoreType.DMA((n, 3))], start, wait)


HBM = pl.BlockSpec(memory_space=pltpu.HBM)
SEM = pl.BlockSpec(memory_space=pltpu.SEMAPHORE)


_IN_FLIGHT = pltpu.CompilerParams(has_side_effects=pltpu.SideEffectType.DATAFLOW_SIDE_EFFECTING)


class _FlatSems:
    def __init__(self, ref, shape):
        self.ref, self.shape = ref, shape

    @property
    def at(self):
        return self

    def __getitem__(self, idx):
        idx = idx if isinstance(idx, tuple) else (idx,)
        flat = 0
        for i, n in zip(idx, self.shape):
            flat = flat * n + i
        return self.ref.at[flat]


def _flat_sem_types(rider):
    return tuple(pltpu.SemaphoreType.DMA((int(np.prod(s.shape)),)) for s in rider.scratch)


def _as_rider_sems(rider, refs):
    return [_FlatSems(r, s.shape) for r, s in zip(refs, rider.scratch)]


def _split_start(rider, name, after=()):
    n_in, n_out, n_sem = len(rider.operands), len(rider.out_shapes), len(rider.scratch)
    n_after = len(after)
    fresh = [j for j in range(n_out) if j not in rider.aliases.values()]
    by_out = {j: i for i, j in rider.aliases.items()}

    def body(*refs):
        r_in = refs[:n_in]
        refs = refs[n_in + n_after:]
        sems = refs[:n_sem]
        thru = refs[n_sem:n_sem + n_in]
        fresh_refs = refs[n_sem + n_in:n_sem + n_in + len(fresh)]
        token = refs[-1]
        r_out = [thru[by_out[j]] if j in by_out else fresh_refs[fresh.index(j)] for j in range(n_out)]
        rider.start(r_in, r_out, _as_rider_sems(rider, sems))
        token[...] = jnp.zeros_like(token)

    res = pl.pallas_call(
        body, name=name,
        out_shape=_flat_sem_types(rider) + tuple(pltpu.HBM(o.shape, o.dtype) for o in rider.operands)
        + tuple(pltpu.HBM(rider.out_shapes[j].shape, rider.out_shapes[j].dtype) for j in fresh)
        + (jax.ShapeDtypeStruct((8, LANES), F32),),
        in_specs=(HBM,) * n_in + (ANY,) * n_after,
        out_specs=(SEM,) * n_sem + (HBM,) * (n_in + len(fresh)) + (pl.BlockSpec(memory_space=pltpu.VMEM),),
        input_output_aliases={i: n_sem + i for i in range(n_in)}, compiler_params=_IN_FLIGHT,
    )(*[pltpu.with_memory_space_constraint(o, pltpu.HBM) for o in rider.operands], *after)
    return (rider, res[:n_sem], res[n_sem:n_sem + n_in], res[n_sem + n_in:-1]), res[-1]


def _split_continue(handles, after, name, phase):
    rider, sems, thru, fresh_arrays = handles
    n_in, n_out, n_sem = len(rider.operands), len(rider.out_shapes), len(rider.scratch)
    fresh = [j for j in range(n_out) if j not in rider.aliases.values()]
    by_out = {j: i for i, j in rider.aliases.items()}
    n_data = n_in + len(fresh)

    def body(*refs):
        r_in = refs[:n_in]
        fresh_refs = refs[n_in:n_data]
        sem_refs = refs[n_data:n_data + n_sem]
        r_out = [r_in[by_out[j]] if j in by_out else fresh_refs[fresh.index(j)] for j in range(n_out)]
        phase(r_in, r_out, _as_rider_sems(rider, sem_refs))

    data = list(thru) + list(fresh_arrays)
    return pl.pallas_call(
        body, name=name, out_shape=tuple(pltpu.HBM(d.shape, d.dtype) for d in data),
        in_specs=(HBM,) * n_data + (SEM,) * n_sem + (ANY,) * len(after), out_specs=(HBM,) * n_data,
        input_output_aliases={i: i for i in range(n_data)}, compiler_params=_IN_FLIGHT,
    )(*data, *sems, *after)


def _split_middle(handles, after, name):
    rider, sems, thru, _ = handles
    res = _split_continue(handles, after, name, rider.middle)
    return rider, sems, res[:len(thru)], res[len(thru):]


def _split_wait(handles, after, name):
    rider = handles[0]
    n_in, n_out = len(rider.operands), len(rider.out_shapes)
    fresh = [j for j in range(n_out) if j not in rider.aliases.values()]
    by_out = {j: i for i, j in rider.aliases.items()}
    res = _split_continue(handles, after, name, rider.wait)
    return [res[by_out[j]] if j in by_out else res[n_in + fresh.index(j)] for j in range(n_out)], res[:n_in]


def _pair_gather_rider(bufs):
    n = len(bufs)

    def copies(r_out, sems):
        x, y, c = _mesh_pos()
        out = []
        for a in range(n):
            send = functools.partial(
                    pltpu.make_async_remote_copy,
                src_ref=r_out[a].at[c], dst_ref=r_out[a].at[c], send_sem=sems[0].at[a],
                recv_sem=sems[1].at[a], device_id=(x, y, 1 - c), device_id_type=MESH)
            recv = functools.partial(
                    pltpu.make_async_remote_copy,
                src_ref=r_out[a].at[1 - c], dst_ref=r_out[a].at[1 - c], send_sem=sems[0].at[a],
                recv_sem=sems[1].at[a], device_id=(x, y, 1 - c), device_id_type=MESH)
            out.append((send, recv))
        return out

    def start(r_in, r_out, sems):
        for send, _ in copies(r_out, sems):
            send().start()

    def wait(r_in, r_out, sems):
        cps = copies(r_out, sems)
        for _, recv in cps:
            recv().wait_recv()
        for send, _ in cps:
            send().wait_send()

    return _Rider(bufs, [jax.ShapeDtypeStruct(b.shape, b.dtype) for b in bufs], {i: i for i in range(n)},
                  [pltpu.SemaphoreType.DMA((n,)), pltpu.SemaphoreType.DMA((n,))], start, wait)


def _add_own_half(g, recv, pos_arr, name):
    nb, rh, cols = g.shape[0], g.shape[-2], g.shape[-1]

    def body(pos_ref, g_ref, r_ref, send_ref, own_ref):
        s = (g_ref[...] + r_ref[...]).astype(BF16)
        send_ref[...] = s

        @pl.when(pl.program_id(0) == pos_ref[1])
        def _():
            own_ref[...] = s

    blk = pl.BlockSpec((None, rh, cols), lambda j, pos_ref: (j, 0, 0))
    g_spec = blk if g.ndim == 3 else pl.BlockSpec((None, None, rh, cols),
                                                   lambda j, pos_ref: (j, pos_ref[0], 0, 0))
    shape = jax.ShapeDtypeStruct((nb, rh, cols), BF16)
    return pl.pallas_call(
        body, name=name,
        grid_spec=pltpu.PrefetchScalarGridSpec(
            num_scalar_prefetch=1, grid=(nb,), in_specs=[g_spec, blk],
            out_specs=[blk, pl.BlockSpec((None, rh, cols), lambda j, pos_ref: (pos_ref[1], 0, 0))]),
        out_shape=[shape, shape], compiler_params=_params(),
    )(pos_arr, g, recv)


def _sum_chips(gath, pos_arr, name):
    nb, rh, cols = gath.shape

    def body(pos_ref, a_ref, b_ref, c_ref, d_ref, o_ref):
        del pos_ref
        o_ref[...] = ((a_ref[...].astype(F32) + b_ref[...].astype(F32)) + c_ref[...].astype(F32)) \
            + d_ref[...].astype(F32)

    tr = rh // 2 if (rh // 2) % 16 == 0 else rh
    specs = [pl.BlockSpec((None, tr, cols), functools.partial(lambda i, pos_ref, j: (j, i, 0), j=j))
             for j in range(nb)]
    return pl.pallas_call(
        body, name=name,
        grid_spec=pltpu.PrefetchScalarGridSpec(
            num_scalar_prefetch=1, grid=(rh // tr,), in_specs=specs,
            out_specs=pl.BlockSpec((None, tr, cols), lambda i, pos_ref: (pos_ref[0], i, 0))),
        out_shape=jax.ShapeDtypeStruct((2, rh, cols), F32), compiler_params=_params(),
    )(pos_arr, gath, gath, gath, gath)


N_DEVICES = 8


def _small_gather_rider(buf):
    def copies(r_out, sems):
        x, y, c = _mesh_pos()
        me = 4 * x + 2 * y + c
        out = []
        for r in range(1, N_DEVICES):
            px = 1 - x if r & 4 else x
            py = 1 - y if r & 2 else y
            pc = 1 - c if r & 1 else c
            out.append(pltpu.make_async_remote_copy(
                src_ref=r_out[0].at[me], dst_ref=r_out[0].at[me], send_sem=sems[0].at[r - 1],
                recv_sem=sems[1].at[r - 1], device_id=(px, py, pc), device_id_type=MESH))
        return out

    def start(r_in, r_out, sems):
        for cp in copies(r_out, sems):
            cp.start()

    def wait(r_in, r_out, sems):
        cps = copies(r_out, sems)
        for cp in cps:
            cp.wait_recv()
        for cp in cps:
            cp.wait_send()

    return _Rider([buf], [jax.ShapeDtypeStruct(buf.shape, buf.dtype)], {0: 0},
                  [pltpu.SemaphoreType.DMA((N_DEVICES - 1,)), pltpu.SemaphoreType.DMA((N_DEVICES - 1,))],
                  start, wait)


def _sum_devices(buf, name):
    def body(b_ref, o_ref):
        acc = b_ref[0]
        for i in range(1, N_DEVICES):
            acc = acc + b_ref[i]
        o_ref[...] = acc

    return _pallas(body, name=name, grid=(1,), in_specs=[pl.BlockSpec(buf.shape, lambda i: (0, 0, 0))],
                   out_specs=pl.BlockSpec(buf.shape[1:], lambda i: (0, 0)),
                   out_shape=jax.ShapeDtypeStruct(buf.shape[1:], F32), operands=[buf])


def _adamw_math(w, g, m, v):
    m = ADAM_B1 * m + (1.0 - ADAM_B1) * g
    v = ADAM_B2 * v + (1.0 - ADAM_B2) * (g * g)
    m_hat = m / (1.0 - ADAM_B1 ** ADAM_STEP)
    v_hat = v / (1.0 - ADAM_B2 ** ADAM_STEP)
    delta = -ADAM_LR * (m_hat / (jnp.sqrt(v_hat) + ADAM_EPS) + ADAM_WD * w)
    return delta, m, v


def _adamw(w, g, m, v, name, after=()):
    r, c = w.shape
    tr = next(t for t in (256, 352, 128, 64) if r % t == 0 and r >= 2 * t)

    def body(w_ref, g_ref, m_ref, v_ref, go_ref, d_ref, mo_ref, vo_ref):
        gv = g_ref[...]
        d, mn, vn = _adamw_math(w_ref[...], gv, m_ref[...], v_ref[...])
        go_ref[...] = gv
        d_ref[...] = d
        mo_ref[...] = mn
        vo_ref[...] = vn

    blk = pl.BlockSpec((tr, c), lambda i: (i, 0))
    return _pallas(body, name=name, grid=(r // tr,), in_specs=[blk] * 4, out_specs=[blk] * 4,
                   out_shape=[jax.ShapeDtypeStruct((r, c), F32)] * 4, operands=[w, g, m, v], after=after)


def _adamw_small(ws, gs, ms, vs, name):
    n = len(ws)

    def body(*refs):
        w_r, g_r, m_r, v_r = refs[:n], refs[n:2 * n], refs[2 * n:3 * n], refs[3 * n:4 * n]
        d_o, m_o, v_o = refs[4 * n:5 * n], refs[5 * n:6 * n], refs[6 * n:7 * n]
        for i in range(n):
            d, mn, vn = _adamw_math(w_r[i][...], g_r[i][...], m_r[i][...], v_r[i][...])
            d_o[i][...] = d
            m_o[i][...] = mn
            v_o[i][...] = vn

    specs = [pl.BlockSpec(w.shape, lambda i: (0, 0)) for w in ws]
    shapes = [jax.ShapeDtypeStruct(w.shape, F32) for w in ws]
    outs = pl.pallas_call(
        body, name=name, grid=(1,), in_specs=specs * 4, out_specs=specs * 3, out_shape=shapes * 3,
        compiler_params=_params(),
    )(*ws, *gs, *ms, *vs)
    return outs[:n], outs[n:2 * n], outs[2 * n:]


BIG = ("w_in", "w_o_attn", "w_pw_conv", "w_out", "w_ffn_in", "w_ffn_out")
ROW_SHARDED = ("w_out", "w_ffn_out")
SMALL = ("norm1_w", "b_gate", "q_norm_w", "k_norm_w", "conv_w", "conv_b", "conv_ln_w", "conv_ln_b", "norm2_w")
ORDER = ("norm1_w", "w_in", "b_gate", "q_norm_w", "k_norm_w", "w_o_attn", "conv_w", "conv_b", "conv_ln_w",
         "conv_ln_b", "w_pw_conv", "w_out", "norm2_w", "w_ffn_in", "w_ffn_out")
PACK_TILE = 8 * LANES


def _pack_small(parts):
    rows = []
    for p in parts:
        flat = p.reshape(-1)
        pad = (-flat.shape[0]) % PACK_TILE
        rows.append(jnp.pad(flat, (0, pad)).reshape(-1, LANES))
    return jnp.concatenate(rows, axis=0)


def _unpack_small(packed, shapes):
    out, row = [], 0
    for shp in shapes:
        size = int(np.prod(shp))
        nrow = -(-size // PACK_TILE) * (PACK_TILE // LANES)
        out.append(packed[row:row + nrow].reshape(-1)[:size].reshape(shp))
        row += nrow
    return out


def kernel(x, positions, norm1_w, w_in, b_gate, q_norm_w, k_norm_w, w_o_attn, conv_w, conv_b, conv_ln_w, conv_ln_b, w_pw_conv, w_out, norm2_w, w_ffn_in, w_ffn_out, loss_target, m_norm1_w, m_w_in, m_b_gate, m_q_norm_w, m_k_norm_w, m_w_o_attn, m_conv_w, m_conv_b, m_conv_ln_w, m_conv_ln_b, m_w_pw_conv, m_w_out, m_norm2_w, m_w_ffn_in, m_w_ffn_out, v_norm1_w, v_w_in, v_b_gate, v_q_norm_w, v_k_norm_w, v_w_o_attn, v_conv_w, v_conv_b, v_conv_ln_w, v_conv_ln_b, v_w_pw_conv, v_w_out, v_norm2_w, v_w_ffn_in, v_w_ffn_out):
    w = dict(norm1_w=norm1_w, w_in=w_in, b_gate=b_gate, q_norm_w=q_norm_w, k_norm_w=k_norm_w, w_o_attn=w_o_attn,
             conv_w=conv_w, conv_b=conv_b, conv_ln_w=conv_ln_w, conv_ln_b=conv_ln_b, w_pw_conv=w_pw_conv,
             w_out=w_out, norm2_w=norm2_w, w_ffn_in=w_ffn_in, w_ffn_out=w_ffn_out)
    m = dict(norm1_w=m_norm1_w, w_in=m_w_in, b_gate=m_b_gate, q_norm_w=m_q_norm_w, k_norm_w=m_k_norm_w,
             w_o_attn=m_w_o_attn, conv_w=m_conv_w, conv_b=m_conv_b, conv_ln_w=m_conv_ln_w,
             conv_ln_b=m_conv_ln_b, w_pw_conv=m_w_pw_conv, w_out=m_w_out, norm2_w=m_norm2_w,
             w_ffn_in=m_w_ffn_in, w_ffn_out=m_w_ffn_out)
    v = dict(norm1_w=v_norm1_w, w_in=v_w_in, b_gate=v_b_gate, q_norm_w=v_q_norm_w, k_norm_w=v_k_norm_w,
             w_o_attn=v_w_o_attn, conv_w=v_conv_w, conv_b=v_conv_b, conv_ln_w=v_conv_ln_w,
             conv_ln_b=v_conv_ln_b, w_pw_conv=v_w_pw_conv, w_out=v_w_out, norm2_w=v_norm2_w,
             w_ffn_in=v_w_ffn_in, w_ffn_out=v_w_ffn_out)
    cx, cy, cc = _mesh_pos()
    chip = 2 * cx + cy

    chip_arr = chip.reshape(1).astype(jnp.int32)
    pos_arr = jnp.stack([cc, chip]).astype(jnp.int32)
    halves = lambda buf: buf.reshape(N_CHIPS, 2, buf.shape[1] // 2, buf.shape[2])
    w_in_buf = halves(_cast_into_slot(w["w_in"][0], chip_arr, BF16, "cast_w_in"))
    small_bufs = [_cast_into_slot(w[n][0], chip_arr, F32, f"slot_{n}") for n in ("conv_w", "b_gate")]
    first_gather, started = _split_start(_gather_both_legs_rider([w_in_buf], small_bufs, NEAR_CHIPS),
                                         "allgather_w_in_near_start")
    late_bufs = [halves(_cast_into_slot(w[n][0], chip_arr, BF16, f"cast_{n}", after=[started]))
                 for n in LATE_GATHER]
    wts = dict(norm1_w=norm1_w, q_norm_w=q_norm_w, k_norm_w=k_norm_w, conv_b=conv_b, conv_ln_w=conv_ln_w,
               conv_ln_b=conv_ln_b, norm2_w=norm2_w)

    loss, grad_x, g, reduced, w_in_in_flight = _forward_backward(
        x[0], positions.reshape(-1, 1), loss_target[0], wts, first_gather, late_bufs, pos_arr)
    grads = {n: b.reshape(-1, b.shape[2]) for n, b in reduced.items()}

    w_in_in_flight, started = w_in_in_flight
    delta, new_m, new_v = {}, {}, {}
    for n in EARLY_REDUCE:
        grads[n], delta[n], new_m[n], new_v[n] = _adamw(w[n][0], grads[n], m[n][0], v[n][0], f"adamw_{n}",
                                                        after=[started])
    small_parts = [loss] + [g[n] for n in SMALL]
    small_shapes = [p.shape for p in small_parts]
    device_arr = (4 * cx + 2 * cy + cc).reshape(1).astype(jnp.int32)
    small_buf = _cast_into_slot(_pack_small(small_parts), device_arr, F32, "slot_small", n_slots=N_DEVICES)

    (by_chip_w_in,), _ = _split_wait(w_in_in_flight, after=[delta[n] for n in EARLY_REDUCE] + [small_buf],
                                     name="grads_w_in_exchange_wait")
    half_w_in = _sum_chips(by_chip_w_in, pos_arr, "grads_chip_sum_w_in")
    shard_w_in, small_buf = _comm_call(
        _riders_together(_pair_gather_rider([half_w_in]), _small_gather_rider(small_buf)),
        "grads_pair_gather_w_in_small_gather")
    summed = _sum_devices(small_buf, "small_sum")
    reduced = _unpack_small(summed, small_shapes)
    loss_total = reduced[0].reshape(())
    for n, r in zip(SMALL, reduced[1:]):
        grads[n] = r
    ch_shard = conv_w.shape[2]
    grads["conv_w"] = lax.dynamic_slice_in_dim(grads["conv_w"], chip * ch_shard, ch_shard, axis=1)
    d_shard = b_gate.shape[2]
    grads["b_gate"] = lax.dynamic_slice_in_dim(grads["b_gate"], chip * d_shard, d_shard, axis=1)

    grads["w_in"], delta["w_in"], new_m["w_in"], new_v["w_in"] = _adamw(
        w["w_in"][0], shard_w_in.reshape(-1, shard_w_in.shape[2]), m["w_in"][0], v["w_in"][0], "adamw_w_in")
    flat2 = lambda a: a.reshape(-1, a.shape[-1])
    d_s, m_s, v_s = _adamw_small([flat2(w[n]) for n in SMALL], [flat2(grads[n]) for n in SMALL],
                                 [flat2(m[n]) for n in SMALL], [flat2(v[n]) for n in SMALL], "adamw_small")
    for i, n in enumerate(SMALL):
        delta[n], new_m[n], new_v[n] = d_s[i], m_s[i], v_s[i]

    shaped = lambda d, n: d[n].reshape(w[n].shape)
    return (loss_total, grad_x[None], *[shaped(grads, n) for n in ORDER], *[shaped(delta, n) for n in ORDER],
            *[shaped(new_m, n) for n in ORDER], *[shaped(new_v, n) for n in ORDER])
```

```python
import functools

import numpy as np
import jax
import jax.numpy as jnp
from jax import lax
from jax.experimental import pallas as pl
from jax.experimental.pallas import tpu as pltpu

F32 = jnp.float32
BF16 = jnp.bfloat16
MESH = pl.DeviceIdType.MESH
ANY = pl.BlockSpec(memory_space=pl.ANY)

HEAD_DIM = 64
N_SLOT_HEADS = 8
DILATIONS = (1, 4, 16)
HALF_SPAN = 64
ROPE_THETA = 500000.0
ROT_DIM = 16
CONV_WIDTH = 31
EPS = 1e-6
NEG_INF = -1e30
ADAM_LR, ADAM_B1, ADAM_B2, ADAM_EPS, ADAM_WD, ADAM_STEP = 0.001, 0.9, 0.999, 1e-08, 0.01, 10

LANES = 128
QBLK = 128
KWIN = QBLK + 2 * HALF_SPAN
VMEM_LIMIT = 44 * 1024 * 1024
N_CHIPS = 4


def _params(**kw):
    return pltpu.CompilerParams(vmem_limit_bytes=VMEM_LIMIT, **kw)


class _Rider:
    def __init__(self, operands, out_shapes, aliases, scratch, start, wait, middle=None):
        self.operands, self.out_shapes, self.aliases = list(operands), list(out_shapes), dict(aliases)
        self.scratch, self.start, self.wait = list(scratch), start, wait
        self.middle = middle


def _riders_together(a, b):
    n_in, n_out, n_sc = len(a.operands), len(a.out_shapes), len(a.scratch)
    aliases = dict(a.aliases)
    aliases.update({n_in + src: n_out + dst for src, dst in b.aliases.items()})

    def start(r_in, r_out, r_sc):
        a.start(r_in[:n_in], r_out[:n_out], r_sc[:n_sc])
        b.start(r_in[n_in:], r_out[n_out:], r_sc[n_sc:])

    def wait(r_in, r_out, r_sc):
        a.wait(r_in[:n_in], r_out[:n_out], r_sc[:n_sc])
        b.wait(r_in[n_in:], r_out[n_out:], r_sc[n_sc:])

    return _Rider(a.operands + b.operands, a.out_shapes + b.out_shapes, aliases, a.scratch + b.scratch, start, wait)


def _pallas(body, *, name, grid, in_specs, out_specs, out_shape, operands, scratch_shapes=(), aliases=None,
            rider=None, after=()):
    single = not isinstance(out_specs, (list, tuple))
    out_specs_l = [out_specs] if single else list(out_specs)
    out_shape_l = [out_shape] if single else list(out_shape)
    aliases = dict(aliases or {})

    def call(fn, all_in_specs, all_out_specs, all_out_shape, all_scratch, all_aliases, all_operands):
        return pl.pallas_call(
            fn, name=name, grid=grid, in_specs=all_in_specs, out_specs=all_out_specs, out_shape=all_out_shape,
            scratch_shapes=all_scratch, input_output_aliases=all_aliases, compiler_params=_params(),
        )(*all_operands)

    if rider is None:
        n_main = len(in_specs)

        def ordered(*refs):
            body(*refs[:n_main], *refs[n_main + len(after):])

        res = call(ordered if after else body, list(in_specs) + [ANY] * len(after), out_specs_l, out_shape_l,
                   list(scratch_shapes), aliases, list(operands) + list(after))
        return res[0] if single else res
    assert not after
    n_in, n_rin = len(in_specs), len(rider.operands)
    n_out, n_rout = len(out_specs_l), len(rider.out_shapes)
    n_sc = len(scratch_shapes)

    def wrapped(*refs):
        main_in, r_in = refs[:n_in], refs[n_in:n_in + n_rin]
        o0 = n_in + n_rin
        main_out, r_out = refs[o0:o0 + n_out], refs[o0 + n_out:o0 + n_out + n_rout]
        s0 = o0 + n_out + n_rout
        main_sc, r_sc = refs[s0:s0 + n_sc], refs[s0 + n_sc:]
        ids = [pl.program_id(d) for d in range(len(grid))]
        first = functools.reduce(jnp.logical_and, [i == 0 for i in ids])
        last = functools.reduce(jnp.logical_and, [i == n - 1 for i, n in zip(ids, grid)])

        @pl.when(first)
        def _():
            rider.start(r_in, r_out, r_sc)

        body(*main_in, *main_out, *main_sc)

        @pl.when(last)
        def _():
            rider.wait(r_in, r_out, r_sc)

    for src, dst in rider.aliases.items():
        aliases[n_in + src] = n_out + dst
    res = call(wrapped, list(in_specs) + [ANY] * n_rin, out_specs_l + [ANY] * n_rout,
               out_shape_l + rider.out_shapes, list(scratch_shapes) + rider.scratch, aliases,
               list(operands) + rider.operands)
    main = res[:n_out]
    return (main[0] if single else main), res[n_out:]


def _matmul(a, b, *, mode, tm, tn, tk, out_dtype, name, b_blocked=False,
            out_blocked=None, cols_outer=False, rider=None, after=()):
    a_shape = a.shape
    if mode == "nn":
        m_dim, k_dim = a_shape
        n_dim = b.shape[0] * b.shape[2] if b_blocked else b.shape[1]
        rows, cols, red = m_dim, n_dim, k_dim
    elif mode == "nt":
        m_dim, n_dim = a_shape
        k_dim = b.shape[1] if b_blocked else b.shape[0]
        rows, cols, red = m_dim, k_dim, n_dim
    else:
        m_dim, k_dim = a_shape
        n_dim = b.shape[1]
        rows, cols, red = k_dim, n_dim, m_dim
    assert rows % tm == 0 and cols % tn == 0 and red % tk == 0, (name, rows, cols, red)
    ni, nj, nk = rows // tm, cols // tn, red // tk

    if mode == "nn":
        a_spec = pl.BlockSpec((tm, tk), lambda i, j, k: (i, k))
        if b_blocked:
            per = b.shape[2] // tn
            b_spec = pl.BlockSpec((None, tk, tn), lambda i, j, k: (j // per, k, j % per))
        else:
            b_spec = pl.BlockSpec((tk, tn), lambda i, j, k: (k, j))
        dims = (((1,), (0,)), ((), ()))
    elif mode == "nt":
        a_spec = pl.BlockSpec((tm, tk), lambda i, j, k: (i, k))
        if b_blocked:
            per = b.shape[2] // tk
            b_spec = pl.BlockSpec((None, tn, tk), lambda i, j, k: (k // per, j, k % per))
        else:
            b_spec = pl.BlockSpec((tn, tk), lambda i, j, k: (j, k))
        dims = (((1,), (1,)), ((), ()))
    else:
        a_spec = pl.BlockSpec((tk, tm), lambda i, j, k: (k, i))
        b_spec = pl.BlockSpec((tk, tn), lambda i, j, k: (k, j))
        dims = (((0,), (0,)), ((), ()))

    if out_blocked:
        per_o = (cols // out_blocked) // tn
        out_spec = pl.BlockSpec((None, tm, tn), lambda i, j, k: (j // per_o, i, j % per_o))
        out_shape = jax.ShapeDtypeStruct((out_blocked, rows, cols // out_blocked), out_dtype)
    else:
        out_spec = pl.BlockSpec((tm, tn), lambda i, j, k: (i, j))
        out_shape = jax.ShapeDtypeStruct((rows, cols), out_dtype)

    def body(a_ref, b_ref, o_ref, *acc):
        prod = lax.dot_general(a_ref[...], b_ref[...], dims, preferred_element_type=F32)
        if nk == 1:
            o_ref[...] = prod.astype(out_dtype)
        else:
            acc_ref, = acc
            k = pl.program_id(2)

            @pl.when(k == 0)
            def _():
                acc_ref[...] = prod

            @pl.when(k > 0)
            def _():
                acc_ref[...] += prod

            @pl.when(k == nk - 1)
            def _():
                o_ref[...] = acc_ref[...].astype(out_dtype)

    scratch = [pltpu.VMEM((tm, tn), F32)] if nk > 1 else []
    grid = (ni, nj, nk)
    if cols_outer:
        swap = lambda spec: pl.BlockSpec(spec.block_shape, lambda j, i, k, f=spec.index_map: f(i, j, k))
        a_spec, b_spec, out_spec, grid = swap(a_spec), swap(b_spec), swap(out_spec), (nj, ni, nk)
    return _pallas(body, name=name, grid=grid, in_specs=[a_spec, b_spec], out_specs=out_spec,
                   out_shape=out_shape, operands=[a, b], scratch_shapes=scratch, rider=rider, after=after)


def _proj_by_slot(h, w_in, order, first, count, name, proj=None, after=()):
    s, k = h.shape
    nb = w_in.shape[2]
    tm = 512
    prev = [] if proj is None else [proj]

    def body(order_ref, h_ref, w_ref, *refs):
        refs[-1][...] = jnp.dot(h_ref[...], w_ref[...], preferred_element_type=F32)

    return pl.pallas_call(
        body, name=name,
        grid_spec=pltpu.PrefetchScalarGridSpec(
            num_scalar_prefetch=1, grid=(count, s // tm),
            in_specs=[pl.BlockSpec((tm, k), lambda j, i, order_ref: (i, 0)),
                      pl.BlockSpec((None, k, nb), lambda j, i, order_ref: (order_ref[first + j], 0, 0))]
            + [ANY] * (len(prev) + len(after)),
            out_specs=pl.BlockSpec((tm, nb), lambda j, i, order_ref: (i, order_ref[first + j]))),
        out_shape=jax.ShapeDtypeStruct((s, N_CHIPS * nb), F32),
        input_output_aliases={3: 0} if prev else {}, compiler_params=_params(),
    )(order, h, w_in, *prev, *after)


def _rmsnorm_fwd(x, w, name):
    s, d = x.shape
    tm = 256

    def body(x_ref, w_ref, o_ref):
        xv = x_ref[...]
        rstd = lax.rsqrt(jnp.mean(xv * xv, axis=-1, keepdims=True) + EPS)
        o_ref[...] = (xv * rstd * w_ref[...]).astype(BF16)

    return pl.pallas_call(
        body, name=name, grid=(s // tm,),
        in_specs=[pl.BlockSpec((tm, d), lambda i: (i, 0)), pl.BlockSpec((1, d), lambda i: (0, 0))],
        out_specs=pl.BlockSpec((tm, d), lambda i: (i, 0)),
        out_shape=jax.ShapeDtypeStruct((s, d), BF16), compiler_params=_params(),
    )(x, w)


def _matmul_nt_rmsnorm_bwd(dz, w_blocked, x, w, dres, name, rider=None, after=()):
    s, d = x.shape
    nk, _, nb = w_blocked.shape
    tm = 512
    nt_dims = (((1,), (1,)), ((), ()))

    def body(a_ref, b_ref, x_ref, w_ref, dres_ref, dx_ref, dxb_ref, dw_ref, acc_ref):
        k, i = pl.program_id(0), pl.program_id(1)
        rows = pl.ds(pl.multiple_of(i * tm, tm), tm)
        prod = lax.dot_general(a_ref[...], b_ref[...], nt_dims, preferred_element_type=F32)

        @pl.when(k == 0)
        def _():
            acc_ref[rows, :] = prod

        @pl.when(jnp.logical_and(k > 0, k < nk - 1))
        def _():
            acc_ref[rows, :] += prod

        @pl.when(k == nk - 1)
        def _():
            xv = x_ref[...]
            rstd = lax.rsqrt(jnp.mean(xv * xv, axis=-1, keepdims=True) + EPS)
            xhat = xv * rstd
            dhv = acc_ref[rows, :] + prod
            g = dhv * w_ref[...]
            dx = rstd * (g - xhat * jnp.mean(g * xhat, axis=-1, keepdims=True)) + dres_ref[...]
            dx_ref[...] = dx
            dxb_ref[...] = dx.astype(BF16)
            part = jnp.sum(dhv * xhat, axis=0, keepdims=True)

            @pl.when(i == 0)
            def _():
                dw_ref[...] = part

            @pl.when(i > 0)
            def _():
                dw_ref[...] += part

    assert nk >= 2
    row = pl.BlockSpec((tm, d), lambda k, i: (jnp.where(k == nk - 1, i, 0), 0))
    vec = pl.BlockSpec((1, d), lambda k, i: (0, 0))
    return _pallas(
        body, name=name, grid=(nk, s // tm),
        in_specs=[pl.BlockSpec((tm, nb), lambda k, i: (i, k)), pl.BlockSpec((None, d, nb), lambda k, i: (k, 0, 0)),
                  row, vec, row],
        out_specs=[row, row, vec],
        out_shape=[jax.ShapeDtypeStruct((s, d), F32), jax.ShapeDtypeStruct((s, d), BF16),
                   jax.ShapeDtypeStruct((1, d), F32)],
        operands=[dz, w_blocked, x, w, dres], scratch_shapes=[pltpu.VMEM((s, d), F32)], rider=rider, after=after)


def _rope_consts():
    lane = np.arange(LANES)
    in_head = lane % HEAD_DIM
    inv_freq = ROPE_THETA ** (-jnp.arange(0, ROT_DIM, 2, dtype=F32) / ROT_DIM)
    invf = jnp.where(jnp.asarray(in_head < ROT_DIM), jnp.tile(inv_freq, LANES // (ROT_DIM // 2)), 0.0)
    m_a = np.where(in_head < ROT_DIM // 2, -1.0, 0.0).astype(np.float32)
    m_b = np.where((in_head >= ROT_DIM // 2) & (in_head < ROT_DIM), 1.0, 0.0).astype(np.float32)
    block_diag = (lane[:, None] // HEAD_DIM == lane[None, :] // HEAD_DIM).astype(np.float32)
    return (invf.reshape(1, LANES).astype(F32), jnp.asarray(m_a).reshape(1, LANES),
            jnp.asarray(m_b).reshape(1, LANES), jnp.asarray(block_diag, dtype=BF16))


def _head_sums(v, bd):
    hi = v.astype(BF16)
    lo = (v - hi.astype(F32)).astype(BF16)
    return jnp.dot(hi, bd, preferred_element_type=F32) + jnp.dot(lo, bd, preferred_element_type=F32)


def _rope_tables(pos_col, consts, name, after=()):
    s = pos_col.shape[0]
    tm = 512
    invf, m_a, m_b, _ = consts

    def body(pos_ref, invf_ref, ma_ref, mb_ref, cos_ref, sa_ref, sb_ref):
        ang = pos_ref[...].astype(F32) * invf_ref[...]
        sin = jnp.sin(ang)
        cos_ref[...] = jnp.cos(ang)
        sa_ref[...] = sin * ma_ref[...]
        sb_ref[...] = sin * mb_ref[...]

    vec = pl.BlockSpec((1, LANES), lambda i: (0, 0))
    tab = pl.BlockSpec((tm, LANES), lambda i: (i, 0))
    return _pallas(body, name=name, grid=(s // tm,), in_specs=[pl.BlockSpec((tm, 1), lambda i: (i, 0)), vec, vec, vec],
                   out_specs=[tab] * 3, out_shape=[jax.ShapeDtypeStruct((s, LANES), F32)] * 3,
                   operands=[pos_col, invf, m_a, m_b], after=after)


def _qk_fwd(proj, rope, qw2, kw2, bd, name, rider=None):
    s = proj.shape[0]
    width = 3 * N_SLOT_HEADS * HEAD_DIM
    tm = 128
    scale = HEAD_DIM ** -0.5

    def body(q_ref, k_ref, cos_ref, sa_ref, sb_ref, qw_ref, kw_ref, bd_ref, qo_ref, ko_ref):
        cos, s_a, s_b = cos_ref[...], sa_ref[...], sb_ref[...]
        bdv = bd_ref[...]
        for src, w_ref, dst, sc in ((q_ref, qw_ref, qo_ref, scale), (k_ref, kw_ref, ko_ref, 1.0)):
            for cb in range(width // LANES):
                cols = slice(cb * LANES, (cb + 1) * LANES)
                t = src[:, cols]
                rstd = lax.rsqrt(_head_sums(t * t, bdv) * (1.0 / HEAD_DIM) + EPS)
                y = t * rstd * w_ref[...]
                r = y * cos + pltpu.roll(y, LANES - 8, axis=1) * s_a + pltpu.roll(y, 8, axis=1) * s_b
                dst[:, cols] = r * sc if sc != 1.0 else r

    vec = pl.BlockSpec((1, LANES), lambda i: (0, 0))
    tab = pl.BlockSpec((tm, LANES), lambda i: (i, 0))
    return _pallas(
        body, name=name, grid=(s // tm,),
        in_specs=[pl.BlockSpec((tm, width), lambda i: (i, 0)), pl.BlockSpec((tm, width), lambda i: (i, 1)),
                  tab, tab, tab, vec, vec, pl.BlockSpec((LANES, LANES), lambda i: (0, 0))],
        out_specs=[pl.BlockSpec((tm, width), lambda i: (i, 0))] * 2,
        out_shape=[jax.ShapeDtypeStruct((s, width), F32)] * 2,
        operands=[proj, proj, *rope, qw2, kw2, bd], rider=rider)


def _qk_bwd(dqn, dkn, dv, da, db, dgl, proj, rope, qw2, kw2, bd, name, rider=None):
    s = proj.shape[0]
    width = 3 * N_SLOT_HEADS * HEAD_DIM
    ch = da.shape[1]
    gate_w = dgl.shape[1]
    out_w = 3 * width + 2 * ch + gate_w
    assert out_w == proj.shape[1]
    tm = 128
    scale = HEAD_DIM ** -0.5

    def body(dq_ref, dk_ref, dv_ref, da_ref, db_ref, dgl_ref, q_ref, k_ref, cos_ref, sa_ref, sb_ref, qw_ref, kw_ref,
             bd_ref, out_ref, dqw_ref, dkw_ref):
        cos, s_a, s_b = cos_ref[...], sa_ref[...], sb_ref[...]
        bdv = bd_ref[...]
        first = pl.program_id(0) == 0
        for src, dsrc, w_ref, col0, dw_ref, sc in ((q_ref, dq_ref, qw_ref, 0, dqw_ref, scale),
                                                   (k_ref, dk_ref, kw_ref, width, dkw_ref, 1.0)):
            dw_acc = jnp.zeros((1, LANES), F32)
            for cb in range(width // LANES):
                cols = slice(cb * LANES, (cb + 1) * LANES)
                t = src[:, cols]
                dr = dsrc[:, cols]
                if sc != 1.0:
                    dr = dr * sc
                dy = dr * cos + pltpu.roll(dr * s_a, 8, axis=1) + pltpu.roll(dr * s_b, LANES - 8, axis=1)
                rstd = lax.rsqrt(_head_sums(t * t, bdv) * (1.0 / HEAD_DIM) + EPS)
                xhat = t * rstd
                g = dy * w_ref[...]
                dt = rstd * (g - xhat * (_head_sums(g * xhat, bdv) * (1.0 / HEAD_DIM)))
                out_ref[:, col0 + cb * LANES: col0 + (cb + 1) * LANES] = dt.astype(BF16)
                dw_acc = dw_acc + jnp.sum(dy * xhat, axis=0, keepdims=True)
            dw_acc = dw_acc + pltpu.roll(dw_acc, HEAD_DIM, axis=1)

            @pl.when(first)
            def _(dw_ref=dw_ref, dw_acc=dw_acc):
                dw_ref[...] = dw_acc

            @pl.when(jnp.logical_not(first))
            def _(dw_ref=dw_ref, dw_acc=dw_acc):
                dw_ref[...] += dw_acc
        out_ref[:, 2 * width: 3 * width] = dv_ref[...].astype(BF16)
        out_ref[:, 3 * width: 3 * width + ch] = da_ref[...]
        out_ref[:, 3 * width + ch: 3 * width + 2 * ch] = db_ref[...]
        out_ref[:, 3 * width + 2 * ch: out_w] = dgl_ref[...]

    vec = pl.BlockSpec((1, LANES), lambda i: (0, 0))
    blk = lambda c: pl.BlockSpec((tm, width), lambda i: (i, c))
    cblk = pl.BlockSpec((tm, ch), lambda i: (i, 0))
    tab = pl.BlockSpec((tm, LANES), lambda i: (i, 0))
    return _pallas(
        body, name=name, grid=(s // tm,),
        in_specs=[blk(0), blk(0), blk(0), cblk, cblk, pl.BlockSpec((tm, gate_w), lambda i: (i, 0)),
                  blk(0), blk(1), tab, tab, tab, vec, vec, pl.BlockSpec((LANES, LANES), lambda i: (0, 0))],
        out_specs=[pl.BlockSpec((tm, out_w), lambda i: (i, 0)), vec, vec],
        out_shape=[jax.ShapeDtypeStruct((s, out_w), BF16)] + [jax.ShapeDtypeStruct((1, LANES), F32)] * 2,
        operands=[dqn, dkn, dv, da, db, dgl, proj, proj, *rope, qw2, kw2, bd], rider=rider)


def _row_chunks(n_rows, fn, chunk=256):
    def step(i, c):
        fn(pl.ds(pl.multiple_of(i * chunk, chunk), chunk))
        return c
    lax.fori_loop(0, n_rows // chunk, step, 0)


def _to_residue_major(dst, src, s, d, dst_off=0, cast=None):
    seq = s // d
    for r in range(d):
        v = src[...] if d == 1 else src[pl.ds(r, seq, stride=d), :]
        dst[dst_off + r * seq: dst_off + (r + 1) * seq, :] = v if cast is None else v.astype(cast)


def _from_residue_major(dst, src, s, d, src_off=0):
    seq = s // d
    for r in range(d):
        v = src[src_off + r * seq: src_off + (r + 1) * seq, :]
        if d == 1:
            dst[...] = v
        else:
            dst[pl.ds(r, seq, stride=d), :] = v


def _band_bias():
    qi = lax.broadcasted_iota(jnp.int32, (QBLK, KWIN), 0)
    kj = lax.broadcasted_iota(jnp.int32, (QBLK, KWIN), 1)
    return jnp.where(jnp.abs(kj - HALF_SPAN - qi) <= HALF_SPAN, 0.0, NEG_INF).astype(F32)


def _range_bias(base, seq):
    kj = lax.broadcasted_iota(jnp.int32, (1, KWIN), 1)
    lo = (base & -seq) - base + HALF_SPAN
    return jnp.where((kj >= lo) & (kj < lo + seq), 0.0, NEG_INF).astype(F32)


def _skewed_blocks(n_blk, produce, consume):
    produce(0, 0)
    for b in range(n_blk):
        consume(b, b % 2)
        if b + 1 < n_blk:
            produce(b + 1, (b + 1) % 2)


def _block_base(b):
    return b * QBLK if isinstance(b, int) else pl.multiple_of(b * QBLK, QBLK)


def _attn_fwd(qn, kn, proj, name, rider=None):
    s = qn.shape[0]
    n_pairs = N_SLOT_HEADS * HEAD_DIM // LANES
    v_col0 = 2 * qn.shape[1] // LANES
    nt_dims = (((1,), (1,)), ((), ()))

    def body(q_ref, k_ref, v_ref, attn_ref, lse_ref, attn_b_ref, q_rm, k_rm, v_rm, acc_rm, m_rm, l_rm,
             acc_p, m_p, l_p, m_run, l_run, acc_run, band, s_buf, m_buf):
        g = pl.program_id(1)
        zpad = jnp.zeros((HALF_SPAN, LANES), BF16)
        k_rm[0:HALF_SPAN, :] = zpad
        k_rm[s + HALF_SPAN: s + 2 * HALF_SPAN, :] = zpad
        v_rm[0:HALF_SPAN, 0:LANES] = zpad
        v_rm[s + HALF_SPAN: s + 2 * HALF_SPAN, 0:LANES] = zpad

        def ones_rows(rows):
            v_rm[pl.ds(rows.start, rows.size), LANES:2 * LANES] = jnp.ones((rows.size, LANES), BF16)

        _row_chunks(s + 2 * HALF_SPAN, ones_rows, chunk=2 * HALF_SPAN)
        band[...] = _band_bias()
        lane = lax.broadcasted_iota(jnp.int32, (QBLK, LANES), 1)
        low = lane < HEAD_DIM
        n_blk = s // QBLK

        for gi, d in enumerate(DILATIONS):
            @pl.when(g == gi)
            def _(gi=gi, d=d):
                seq = s // d
                _to_residue_major(q_rm, q_ref, s, d, cast=BF16)
                _to_residue_major(k_rm, k_ref, s, d, dst_off=HALF_SPAN, cast=BF16)
                _to_residue_major(v_rm.at[:, 0:LANES], v_ref, s, d, dst_off=HALF_SPAN, cast=BF16)

                def scores(b, slot):
                    base = _block_base(b)
                    q = q_rm[pl.ds(base, QBLK), :]
                    zero = jnp.zeros_like(q)
                    q2 = jnp.concatenate([jnp.where(low, q, zero), jnp.where(low, zero, q)], axis=0)
                    sc = lax.dot_general(q2, k_rm[pl.ds(base, KWIN), :], nt_dims, preferred_element_type=F32)
                    bias = band[...] + _range_bias(base, seq)
                    for hh in range(2):
                        rows = slice(hh * QBLK, (hh + 1) * QBLK)
                        sh = sc[rows, :] + bias
                        s_buf[slot, rows, :] = sh
                        m_buf[slot, rows, :] = jnp.broadcast_to(jnp.max(sh, axis=-1, keepdims=True), (QBLK, LANES))

                def outputs(b, slot):
                    base = _block_base(b)
                    sv = s_buf[slot]
                    mb = m_buf[slot]
                    p = jnp.exp(jnp.concatenate([sv[:, 0:LANES] - mb, sv[:, LANES:2 * LANES] - mb], axis=1))
                    pv = jnp.dot(p.astype(BF16), v_rm[pl.ds(base, KWIN), :], preferred_element_type=F32)
                    rows = pl.ds(base, QBLK)
                    acc_rm[rows, :] = jnp.where(low, pv[0:QBLK, 0:LANES], pv[QBLK:2 * QBLK, 0:LANES])
                    l_rm[rows, :] = jnp.where(low, pv[0:QBLK, LANES:2 * LANES], pv[QBLK:2 * QBLK, LANES:2 * LANES])
                    m_rm[rows, :] = jnp.where(low, mb[0:QBLK, :], mb[QBLK:2 * QBLK, :])

                _skewed_blocks(n_blk, scores, outputs)
                if d == 1:
                    src = (acc_rm, m_rm, l_rm)
                else:
                    for dst_, src_ in ((acc_p, acc_rm), (m_p, m_rm), (l_p, l_rm)):
                        _from_residue_major(dst_, src_, s, d)
                    src = (acc_p, m_p, l_p)

                def combine(rows):
                    a_g, m_g, l_g = src[0][rows, :], src[1][rows, :], src[2][rows, :]
                    if gi == 0:
                        m_new, l_new, a_new = m_g, l_g, a_g
                    else:
                        m_old = m_run[rows, :]
                        m_new = jnp.maximum(m_old, m_g)
                        w_old = jnp.exp(m_old - m_new)
                        w_g = jnp.exp(m_g - m_new)
                        l_new = l_run[rows, :] * w_old + l_g * w_g
                        a_new = acc_run[rows, :] * w_old + a_g * w_g
                    if gi == len(DILATIONS) - 1:
                        out = a_new / l_new
                        attn_ref[rows, :] = out
                        attn_b_ref[rows, :] = out.astype(BF16)
                        lse_ref[rows, :] = m_new + jnp.log(l_new)
                    else:
                        m_run[rows, :] = m_new
                        l_run[rows, :] = l_new
                        acc_run[rows, :] = a_new

                _row_chunks(s, combine)

    qk_spec = pl.BlockSpec((s, LANES), lambda hp, g: (0, g * n_pairs + hp))
    v_spec = pl.BlockSpec((s, LANES), lambda hp, g: (0, v_col0 + g * n_pairs + hp))
    o_spec = pl.BlockSpec((s, LANES), lambda hp, g: (0, hp))
    f32buf = pltpu.VMEM((s, LANES), F32)
    return _pallas(
        body, name=name, grid=(n_pairs, len(DILATIONS)), in_specs=[qk_spec, qk_spec, v_spec],
        out_specs=[o_spec, o_spec, o_spec],
        out_shape=[jax.ShapeDtypeStruct((s, n_pairs * LANES), F32)] * 2
        + [jax.ShapeDtypeStruct((s, n_pairs * LANES), BF16)],
        operands=[qn, kn, proj],
        scratch_shapes=[pltpu.VMEM((s, LANES), BF16), pltpu.VMEM((s + 2 * HALF_SPAN, LANES), BF16),
                        pltpu.VMEM((s + 2 * HALF_SPAN, 2 * LANES), BF16)] + [f32buf] * 9
        + [pltpu.VMEM((QBLK, KWIN), F32), pltpu.VMEM((2, 2 * QBLK, KWIN), F32),
           pltpu.VMEM((2, 2 * QBLK, LANES), F32)],
        rider=rider)


def _attn_bwd(qn, kn, proj, dattn, attn, lse, bd, name, rider=None):
    s = qn.shape[0]
    n_pairs = N_SLOT_HEADS * HEAD_DIM // LANES
    v_col0 = 2 * qn.shape[1] // LANES
    nt_dims = (((1,), (1,)), ((), ()))
    tn_dims = (((0,), (0,)), ((), ()))
    spad = s + 2 * HALF_SPAN

    def body(q_ref, k_ref, v_ref, do_ref, o_ref, lse_ref, bd_ref, dq_ref, dk_ref, dv_ref,
             q_rm, k_rm, v_rm, do_rm, lse0_rm, lse1_rm, dd0_rm, dd1_rm, dq_rm, dk_rm, dv_rm,
             lse0_p, lse1_p, dd0_p, dd1_p, band, p_buf, ds_buf):
        g = pl.program_id(1)
        zpad = jnp.zeros((HALF_SPAN, LANES), BF16)
        for buf in (k_rm, v_rm):
            buf[0:HALF_SPAN, :] = zpad
            buf[s + HALF_SPAN: spad, :] = zpad
        zf = jnp.zeros((HALF_SPAN, LANES), F32)
        for buf in (dk_rm, dv_rm):
            buf[0:HALF_SPAN, :] = zf
            buf[s + HALF_SPAN: spad, :] = zf
        band[...] = _band_bias()

        def clear(rows):
            z = jnp.zeros((rows.size, LANES), F32)
            dk_rm[pl.ds(rows.start + HALF_SPAN, rows.size), :] = z
            dv_rm[pl.ds(rows.start + HALF_SPAN, rows.size), :] = z

        _row_chunks(s, clear)

        def prepare(rows):
            lo = lax.broadcasted_iota(jnp.int32, (rows.size, LANES), 1) < HEAD_DIM
            dsum = _head_sums(do_ref[rows, :] * o_ref[rows, :], bd_ref[...])
            dswap = pltpu.roll(dsum, HEAD_DIM, axis=1)
            dd0_p[rows, :] = jnp.where(lo, dsum, dswap)
            dd1_p[rows, :] = jnp.where(lo, dswap, dsum)
            lv = lse_ref[rows, :]
            lswap = pltpu.roll(lv, HEAD_DIM, axis=1)
            lse0_p[rows, :] = jnp.where(lo, lv, lswap)
            lse1_p[rows, :] = jnp.where(lo, lswap, lv)

        @pl.when(g == 0)
        def _():
            _row_chunks(s, prepare)
        lane = lax.broadcasted_iota(jnp.int32, (QBLK, LANES), 1)
        low = lane < HEAD_DIM
        n_blk = s // QBLK

        def stacked(ref, rows):
            val = ref[rows, :]
            zero = jnp.zeros_like(val)
            return jnp.concatenate([jnp.where(low, val, zero), jnp.where(low, zero, val)], axis=0)

        for gi, d in enumerate(DILATIONS):
            @pl.when(g == gi)
            def _(d=d):
                seq = s // d
                _to_residue_major(q_rm, q_ref, s, d, cast=BF16)
                _to_residue_major(k_rm, k_ref, s, d, dst_off=HALF_SPAN, cast=BF16)
                _to_residue_major(v_rm, v_ref, s, d, dst_off=HALF_SPAN, cast=BF16)
                _to_residue_major(do_rm, do_ref, s, d, cast=BF16)
                for dst_, src_ in ((lse0_rm, lse0_p), (lse1_rm, lse1_p), (dd0_rm, dd0_p), (dd1_rm, dd1_p)):
                    _to_residue_major(dst_, src_, s, d)

                def scores(b, slot):
                    base = _block_base(b)
                    rows = pl.ds(base, QBLK)
                    win = pl.ds(base, KWIN)
                    sc = lax.dot_general(stacked(q_rm, rows), k_rm[win, :], nt_dims, preferred_element_type=F32)
                    dp = lax.dot_general(stacked(do_rm, rows), v_rm[win, :], nt_dims, preferred_element_type=F32)
                    bias = band[...] + _range_bias(base, seq)
                    for hh, (lse_r, dd_r) in enumerate(((lse0_rm, dd0_rm), (lse1_rm, dd1_rm))):
                        r = slice(hh * QBLK, (hh + 1) * QBLK)
                        lse_h = lse_r[rows, :]
                        dd_h = dd_r[rows, :]
                        sh = sc[r, :] + bias
                        p = jnp.exp(jnp.concatenate([sh[:, 0:LANES] - lse_h, sh[:, LANES:KWIN] - lse_h], axis=1))
                        dph = dp[r, :]
                        ds = p * jnp.concatenate([dph[:, 0:LANES] - dd_h, dph[:, LANES:KWIN] - dd_h], axis=1)
                        p_buf[slot, r, :] = p.astype(BF16)
                        ds_buf[slot, r, :] = ds.astype(BF16)

                def grads(b, slot):
                    base = _block_base(b)
                    rows = pl.ds(base, QBLK)
                    win = pl.ds(base, KWIN)
                    p = p_buf[slot]
                    ds = ds_buf[slot]
                    dq2 = jnp.dot(ds, k_rm[win, :], preferred_element_type=F32)
                    dq_rm[rows, :] = jnp.where(low, dq2[0:QBLK, :], dq2[QBLK:2 * QBLK, :])
                    dk_rm[win, :] += lax.dot_general(ds, stacked(q_rm, rows), tn_dims, preferred_element_type=F32)
                    dv_rm[win, :] += lax.dot_general(p, stacked(do_rm, rows), tn_dims, preferred_element_type=F32)

                _skewed_blocks(n_blk, scores, grads)
                _from_residue_major(dq_ref, dq_rm, s, d)
                _from_residue_major(dk_ref, dk_rm, s, d, src_off=HALF_SPAN)
                _from_residue_major(dv_ref, dv_rm, s, d, src_off=HALF_SPAN)

    qk_spec = pl.BlockSpec((s, LANES), lambda hp, g: (0, g * n_pairs + hp))
    v_spec = pl.BlockSpec((s, LANES), lambda hp, g: (0, v_col0 + g * n_pairs + hp))
    o_spec = pl.BlockSpec((s, LANES), lambda hp, g: (0, hp))
    width = qn.shape[1]
    f32buf = pltpu.VMEM((s, LANES), F32)
    f32pad = pltpu.VMEM((spad, LANES), F32)
    return _pallas(
        body, name=name, grid=(n_pairs, len(DILATIONS)),
        in_specs=[qk_spec, qk_spec, v_spec, o_spec, o_spec, o_spec,
                  pl.BlockSpec((LANES, LANES), lambda hp, g: (0, 0))],
        out_specs=[qk_spec, qk_spec, qk_spec],
        out_shape=[jax.ShapeDtypeStruct((s, width), F32)] * 3,
        operands=[qn, kn, proj, dattn, attn, lse, bd],
        scratch_shapes=[pltpu.VMEM((s, LANES), BF16), pltpu.VMEM((spad, LANES), BF16),
                        pltpu.VMEM((spad, LANES), BF16), pltpu.VMEM((s, LANES), BF16),
                        f32buf, f32buf, f32buf, f32buf, f32buf, f32pad, f32pad,
                        f32buf, f32buf, f32buf, f32buf, pltpu.VMEM((QBLK, KWIN), F32),
                        pltpu.VMEM((2, 2 * QBLK, KWIN), BF16), pltpu.VMEM((2, 2 * QBLK, KWIN), BF16)],
        rider=rider)


CONV_PAD = 16


def _conv_fwd(proj, conv_w, conv_b, col0, name, rider=None):
    s = proj.shape[0]
    ch = conv_w.shape[1]
    nblk = ch // LANES
    a0 = col0 // LANES
    tr = 256
    shift = CONV_PAD - (CONV_WIDTH - 1) // 2

    def body(a_ref, b_ref, w_ref, bias_ref, u0_ref, uc_ref, pad):
        z = jnp.zeros((CONV_PAD, LANES), F32)
        pad[0:CONV_PAD, :] = z
        pad[s + CONV_PAD: s + 2 * CONV_PAD, :] = z

        def glu(rows):
            u0 = a_ref[rows, :] * jax.nn.sigmoid(b_ref[rows, :])
            u0_ref[rows, :] = u0
            pad[pl.ds(rows.start + CONV_PAD, rows.size), :] = u0

        _row_chunks(s, glu)
        for t in range(0, s, tr):
            acc = jnp.broadcast_to(bias_ref[...], (tr, LANES))
            for k in range(CONV_WIDTH):
                acc = acc + w_ref[k:k + 1, :] * pad[t + k + shift: t + k + shift + tr, :]
            uc_ref[t:t + tr, :] = acc

    return _pallas(
        body, name=name, grid=(nblk,),
        in_specs=[pl.BlockSpec((s, LANES), lambda c: (0, a0 + c)),
                  pl.BlockSpec((s, LANES), lambda c: (0, a0 + nblk + c)),
                  pl.BlockSpec((CONV_WIDTH, LANES), lambda c: (0, c)),
                  pl.BlockSpec((1, LANES), lambda c: (0, c))],
        out_specs=[pl.BlockSpec((s, LANES), lambda c: (0, c))] * 2,
        out_shape=[jax.ShapeDtypeStruct((s, ch), F32)] * 2, operands=[proj, proj, conv_w, conv_b],
        scratch_shapes=[pltpu.VMEM((s + 2 * CONV_PAD, LANES), F32)], rider=rider)


def _ln_silu_fwd(uc, ln_w, ln_b, name):
    s, ch = uc.shape
    tm = 256

    def body(u_ref, w_ref, b_ref, o_ref):
        u = u_ref[...]
        mu = jnp.mean(u, axis=-1, keepdims=True)
        xc = u - mu
        rstd = lax.rsqrt(jnp.mean(xc * xc, axis=-1, keepdims=True) + EPS)
        z = xc * rstd * w_ref[...] + b_ref[...]
        o_ref[...] = (z * jax.nn.sigmoid(z)).astype(BF16)

    row = pl.BlockSpec((tm, ch), lambda i: (i, 0))
    vec = pl.BlockSpec((1, ch), lambda i: (0, 0))
    return pl.pallas_call(
        body, name=name, grid=(s // tm,), in_specs=[row, vec, vec], out_specs=row,
        out_shape=jax.ShapeDtypeStruct((s, ch), BF16), compiler_params=_params(),
    )(uc, ln_w, ln_b)


def _ln_silu_bwd(du3, uc, ln_w, ln_b, name):
    s, ch = uc.shape
    tm = 256

    def body(d_ref, u_ref, w_ref, b_ref, du_ref, dw_ref, db_ref):
        u = u_ref[...]
        mu = jnp.mean(u, axis=-1, keepdims=True)
        xc = u - mu
        rstd = lax.rsqrt(jnp.mean(xc * xc, axis=-1, keepdims=True) + EPS)
        xhat = xc * rstd
        z = xhat * w_ref[...] + b_ref[...]
        sg = jax.nn.sigmoid(z)
        dz = d_ref[...] * (sg * (1.0 + z * (1.0 - sg)))
        dxh = dz * w_ref[...]
        du_ref[...] = rstd * (dxh - jnp.mean(dxh, axis=-1, keepdims=True)
                              - xhat * jnp.mean(dxh * xhat, axis=-1, keepdims=True))
        pw = jnp.sum(dz * xhat, axis=0, keepdims=True)
        pb = jnp.sum(dz, axis=0, keepdims=True)
        first = pl.program_id(0) == 0

        @pl.when(first)
        def _():
            dw_ref[...] = pw
            db_ref[...] = pb

        @pl.when(jnp.logical_not(first))
        def _():
            dw_ref[...] += pw
            db_ref[...] += pb

    row = pl.BlockSpec((tm, ch), lambda i: (i, 0))
    vec = pl.BlockSpec((1, ch), lambda i: (0, 0))
    return pl.pallas_call(
        body, name=name, grid=(s // tm,), in_specs=[row, row, vec, vec], out_specs=[row, vec, vec],
        out_shape=[jax.ShapeDtypeStruct((s, ch), F32), jax.ShapeDtypeStruct((1, ch), F32),
                   jax.ShapeDtypeStruct((1, ch), F32)],
        compiler_params=_params(),
    )(du3, uc, ln_w, ln_b)


def _conv_bwd(duc, u0, proj, conv_w, col0, name, rider=None):
    s = proj.shape[0]
    ch = conv_w.shape[1]
    nblk = ch // LANES
    a0 = col0 // LANES
    tr = 256
    half = (CONV_WIDTH - 1) // 2
    shift = CONV_PAD - half

    def body(duc_ref, u0_ref, a_ref, b_ref, w_ref, da_ref, db_ref, dw_ref, dbias_ref, pad_d, pad_u):
        z = jnp.zeros((CONV_PAD, LANES), F32)
        for buf in (pad_d, pad_u):
            buf[0:CONV_PAD, :] = z
            buf[s + CONV_PAD: s + 2 * CONV_PAD, :] = z

        def fill(rows):
            dst = pl.ds(rows.start + CONV_PAD, rows.size)
            pad_d[dst, :] = duc_ref[rows, :]
            pad_u[dst, :] = u0_ref[rows, :]

        _row_chunks(s, fill)
        dw_acc = [jnp.zeros((8, LANES), F32) for _ in range(CONV_WIDTH)]
        dbias_acc = jnp.zeros((8, LANES), F32)
        for t in range(0, s, tr):
            d_t = duc_ref[t:t + tr, :]
            dbias_acc = dbias_acc + jnp.sum(d_t.reshape(tr // 8, 8, LANES), axis=0)
            du0 = jnp.zeros((tr, LANES), F32)
            for k in range(CONV_WIDTH):
                du0 = du0 + w_ref[k:k + 1, :] * pad_d[t - k + half + CONV_PAD: t - k + half + CONV_PAD + tr, :]
                prod = d_t * pad_u[t + k + shift: t + k + shift + tr, :]
                dw_acc[k] = dw_acc[k] + jnp.sum(prod.reshape(tr // 8, 8, LANES), axis=0)
            av = a_ref[t:t + tr, :]
            sg = jax.nn.sigmoid(b_ref[t:t + tr, :])
            da_ref[t:t + tr, :] = (du0 * sg).astype(BF16)
            db_ref[t:t + tr, :] = (du0 * av * sg * (1.0 - sg)).astype(BF16)
        for k in range(CONV_WIDTH):
            dw_ref[k:k + 1, :] = jnp.sum(dw_acc[k], axis=0, keepdims=True)
        dbias_ref[...] = jnp.sum(dbias_acc, axis=0, keepdims=True)

    col = lambda off: pl.BlockSpec((s, LANES), lambda c: (0, off + c))
    return _pallas(
        body, name=name, grid=(nblk,),
        in_specs=[col(0), col(0), col(a0), col(a0 + nblk),
                  pl.BlockSpec((CONV_WIDTH, LANES), lambda c: (0, c))],
        out_specs=[col(0), col(0), pl.BlockSpec((CONV_WIDTH, LANES), lambda c: (0, c)),
                   pl.BlockSpec((1, LANES), lambda c: (0, c))],
        out_shape=[jax.ShapeDtypeStruct((s, ch), BF16)] * 2
        + [jax.ShapeDtypeStruct((CONV_WIDTH, ch), F32), jax.ShapeDtypeStruct((1, ch), F32)],
        operands=[duc, u0, proj, proj, conv_w],
        scratch_shapes=[pltpu.VMEM((s + 2 * CONV_PAD, LANES), F32)] * 2, rider=rider)


def _mix_out_proj(attn_b, u3, w_o, w_pw, proj, bg, col0, w_out, x, norm_w, name, rider=None):
    s, d = x.shape
    k = attn_b.shape[1]
    tm = 256
    half = d // 2
    assert col0 % half == 0
    c0 = col0 // half

    def body(a_ref, u_ref, wo_ref, wp_ref, a0_ref, a1_ref, b0_ref, b1_ref, bias_ref, w_ref, x_ref, nw_ref,
             ya_ref, yb_ref, mixed_ref, x1_ref, h2_ref):
        mixed = None
        for br, (src_ref, wb_ref, lo_ref, hi_ref, y_ref) in enumerate(((a_ref, wo_ref, a0_ref, a1_ref, ya_ref),
                                                                       (u_ref, wp_ref, b0_ref, b1_ref, yb_ref))):
            src = src_ref[...]
            y = jnp.concatenate([jnp.dot(src, wb_ref[j], preferred_element_type=F32) for j in range(N_CHIPS)],
                                axis=1)
            y_ref[...] = y
            logits = jnp.concatenate([lo_ref[...], hi_ref[...]], axis=1)
            part = jax.nn.sigmoid(logits + bias_ref[br]) * y
            mixed = part if mixed is None else mixed + part
        mixed = mixed.astype(BF16)
        mixed_ref[...] = mixed
        x1 = x_ref[...] + jnp.dot(mixed, w_ref[...], preferred_element_type=F32)
        x1_ref[...] = x1
        rstd = lax.rsqrt(jnp.mean(x1 * x1, axis=-1, keepdims=True) + EPS)
        h2_ref[...] = (x1 * rstd * nw_ref[...]).astype(BF16)

    row = pl.BlockSpec((tm, d), lambda i: (i, 0))
    src_row = pl.BlockSpec((tm, k), lambda i: (i, 0))
    blocks = pl.BlockSpec(w_o.shape, lambda i: (0, 0, 0))
    logit_blk = lambda j: pl.BlockSpec((tm, half), functools.partial(lambda i, j: (i, c0 + j), j=j))
    return _pallas(
        body, name=name, grid=(s // tm,),
        in_specs=[src_row, src_row, blocks, blocks, logit_blk(0), logit_blk(1), logit_blk(2), logit_blk(3),
                  pl.BlockSpec((2, 1, d), lambda i: (0, 0, 0)), pl.BlockSpec((d, d), lambda i: (0, 0)), row,
                  pl.BlockSpec((1, d), lambda i: (0, 0))],
        out_specs=[row] * 5,
        out_shape=[jax.ShapeDtypeStruct((s, d), F32), jax.ShapeDtypeStruct((s, d), F32),
                   jax.ShapeDtypeStruct((s, d), BF16), jax.ShapeDtypeStruct((s, d), F32),
                   jax.ShapeDtypeStruct((s, d), BF16)],
        operands=[attn_b, u3, w_o, w_pw, proj, proj, proj, proj, bg, w_out, x, norm_w], rider=rider)


def _out_proj_bwd_gates(dx1, w_out, proj, bg, y_a, y_b, w_o, w_pw, col0, name, after=()):
    s, d = y_a.shape
    n_blk, k, blk = w_o.shape
    tm = 256
    half = d // 2
    assert col0 % half == 0
    c0 = col0 // half
    nt_dims = (((1,), (1,)), ((), ()))

    def body(dx_ref, w_ref, a0_ref, a1_ref, b0_ref, b1_ref, bias_ref, ya_ref, yb_ref, wo_ref, wp_ref,
             dgl_ref, dya_ref, dyb_ref, db_ref, da_ref, du_ref):
        dm = lax.dot_general(dx_ref[...], w_ref[...], nt_dims, preferred_element_type=F32)
        parts = []
        for br, (lo_ref, hi_ref, y_ref, dy_ref, wb_ref, dsrc_ref) in enumerate((
                (a0_ref, a1_ref, ya_ref, dya_ref, wo_ref, da_ref), (b0_ref, b1_ref, yb_ref, dyb_ref, wp_ref, du_ref))):
            logits = jnp.concatenate([lo_ref[...], hi_ref[...]], axis=1)
            gate = jax.nn.sigmoid(logits + bias_ref[br])
            dy = (dm * gate).astype(BF16)
            dy_ref[...] = dy
            dsrc = lax.dot_general(dy[:, 0:blk], wb_ref[0], nt_dims, preferred_element_type=F32)
            for j in range(1, n_blk):
                dsrc = dsrc + lax.dot_general(dy[:, j * blk:(j + 1) * blk], wb_ref[j], nt_dims,
                                              preferred_element_type=F32)
            dsrc_ref[...] = dsrc
            dgl = dm * y_ref[...] * gate * (1.0 - gate)
            dgl_ref[:, br * d:(br + 1) * d] = dgl.astype(BF16)
            parts.append(jnp.sum(dgl, axis=0, keepdims=True))
        part = jnp.concatenate(parts, axis=0)
        first = pl.program_id(0) == 0

        @pl.when(first)
        def _():
            db_ref[...] = part

        @pl.when(jnp.logical_not(first))
        def _():
            db_ref[...] += part

    row = pl.BlockSpec((tm, d), lambda i: (i, 0))
    logit_blk = lambda k: pl.BlockSpec((tm, half), functools.partial(lambda i, k: (i, c0 + k), k=k))
    blocks = pl.BlockSpec(w_o.shape, lambda i: (0, 0, 0))
    src_row = pl.BlockSpec((tm, k), lambda i: (i, 0))
    return _pallas(
        body, name=name, grid=(s // tm,),
        in_specs=[row, pl.BlockSpec((d, d), lambda i: (0, 0)), logit_blk(0), logit_blk(1), logit_blk(2),
                  logit_blk(3), pl.BlockSpec((2, 1, d), lambda i: (0, 0, 0)), row, row, blocks, blocks],
        out_specs=[pl.BlockSpec((tm, 2 * d), lambda i: (i, 0)), row, row, pl.BlockSpec((2, d), lambda i: (0, 0)),
                   src_row, src_row],
        out_shape=[jax.ShapeDtypeStruct((s, 2 * d), BF16), jax.ShapeDtypeStruct((s, d), BF16),
                   jax.ShapeDtypeStruct((s, d), BF16), jax.ShapeDtypeStruct((2, d), F32),
                   jax.ShapeDtypeStruct((s, k), F32), jax.ShapeDtypeStruct((s, k), F32)],
        operands=[dx1, w_out, proj, proj, proj, proj, bg, y_a, y_b, w_o, w_pw], after=after)


def _ffn_in_swiglu(h2, w_blocked, name):
    s, k = h2.shape
    nblk, _, tn = w_blocked.shape
    ff = nblk // 2 * tn
    tm = 512

    def body(a_ref, wg_ref, wu_ref, g_ref, u_ref, act_ref):
        a = a_ref[...]
        gt = jnp.dot(a, wg_ref[...], preferred_element_type=F32)
        up = jnp.dot(a, wu_ref[...], preferred_element_type=F32)
        g_ref[...] = gt
        u_ref[...] = up
        act_ref[...] = (gt * jax.nn.sigmoid(gt) * up).astype(BF16)

    out = pl.BlockSpec((tm, tn), lambda j, i: (i, j))
    return pl.pallas_call(
        body, name=name, grid=(nblk // 2, s // tm),
        in_specs=[pl.BlockSpec((tm, k), lambda j, i: (i, 0)),
                  pl.BlockSpec((None, k, tn), lambda j, i: (j, 0, 0)),
                  pl.BlockSpec((None, k, tn), lambda j, i: (nblk // 2 + j, 0, 0))],
        out_specs=[out, out, out],
        out_shape=[jax.ShapeDtypeStruct((s, ff), F32), jax.ShapeDtypeStruct((s, ff), F32),
                   jax.ShapeDtypeStruct((s, ff), BF16)],
        compiler_params=_params(),
    )(h2, w_blocked, w_blocked)


def _ffn_out_bwd_swiglu(dy, w_ffn_out, gate, up, name, rider=None):
    s, d = dy.shape
    ff = gate.shape[1]
    tm = 256
    nt_dims = (((1,), (1,)), ((), ()))

    def body(dy_ref, w_ref, g_ref, u_ref, o_ref):
        dv = lax.dot_general(dy_ref[...], w_ref[...], nt_dims, preferred_element_type=F32)
        gt = g_ref[...]
        sg = jax.nn.sigmoid(gt)
        o_ref[:, 0:ff] = (dv * u_ref[...] * (sg * (1.0 + gt * (1.0 - sg)))).astype(BF16)
        o_ref[:, ff:2 * ff] = (dv * gt * sg).astype(BF16)

    row = pl.BlockSpec((tm, ff), lambda i: (i, 0))
    return _pallas(
        body, name=name, grid=(s // tm,),
        in_specs=[pl.BlockSpec((tm, d), lambda i: (i, 0)), pl.BlockSpec((ff, d), lambda i: (0, 0)), row, row],
        out_specs=pl.BlockSpec((tm, 2 * ff), lambda i: (i, 0)),
        out_shape=jax.ShapeDtypeStruct((s, 2 * ff), BF16), operands=[dy, w_ffn_out, gate, up], rider=rider)


def _ffn_out_loss(act, w_ffn_out, x1, target, name):
    s, k = act.shape
    d = w_ffn_out.shape[1]
    tm = 512

    def body(a_ref, w_ref, x1_ref, t_ref, dy_ref, dyb_ref, loss_ref, acc):
        y = x1_ref[...] + jnp.dot(a_ref[...], w_ref[...], preferred_element_type=F32)
        diff = y - t_ref[...]
        dy = diff * (1.0 / d)
        dy_ref[...] = dy
        dyb_ref[...] = dy.astype(BF16)
        part = jnp.sum((diff * diff).reshape(tm // 8, 8, d), axis=0)
        i = pl.program_id(0)

        @pl.when(i == 0)
        def _():
            acc[...] = part

        @pl.when(i > 0)
        def _():
            acc[...] += part

        @pl.when(i == pl.num_programs(0) - 1)
        def _():
            loss_ref[...] = (0.5 / d) * jnp.sum(jnp.sum(acc[...], axis=1, keepdims=True), axis=0, keepdims=True)

    row = pl.BlockSpec((tm, d), lambda i: (i, 0))
    return pl.pallas_call(
        body, name=name, grid=(s // tm,),
        in_specs=[pl.BlockSpec((tm, k), lambda i: (i, 0)), pl.BlockSpec((k, d), lambda i: (0, 0)), row, row],
        out_specs=[row, row, pl.BlockSpec((1, 1), lambda i: (0, 0))],
        out_shape=[jax.ShapeDtypeStruct((s, d), F32), jax.ShapeDtypeStruct((s, d), BF16),
                   jax.ShapeDtypeStruct((1, 1), F32)],
        scratch_shapes=[pltpu.VMEM((8, d), F32)], compiler_params=_params(),
    )(act, w_ffn_out, x1, target)


LATE_GATHER = ("w_o_attn", "w_pw_conv", "w_out", "w_ffn_in", "w_ffn_out")
EARLY_REDUCE = LATE_GATHER


def _blocks_by_half(g):
    if g.ndim == 2:
        g = g.reshape(N_CHIPS, g.shape[0] // N_CHIPS, g.shape[1])
    return g.reshape(N_CHIPS, 2, g.shape[1] // 2, g.shape[2])


def _forward_backward(x, pos_col, target, wts, first_gather, late_bufs, pos_arr):
    wts = dict(wts)
    consts = _rope_consts()
    bd = consts[3]
    qw2 = jnp.tile(wts["q_norm_w"], (1, LANES // HEAD_DIM))
    kw2 = jnp.tile(wts["k_norm_w"], (1, LANES // HEAD_DIM))
    qkv_w = 3 * N_SLOT_HEADS * HEAD_DIM
    conv_col0 = 3 * qkv_w

    h = _rmsnorm_fwd(x, wts["norm1_w"], "rms1_fwd")
    slot_order = jnp.bitwise_xor(pos_arr[1], jnp.asarray([0, 2, 1, 3], jnp.int32))
    blocked = lambda buf: buf.reshape(N_CHIPS, -1, buf.shape[3])
    near = first_gather
    rope = _rope_tables(pos_col, consts, "rope_tables", after=list(late_bufs))
    proj = _proj_by_slot(h, blocked(near[2][0]), slot_order, 0, 1, "mm_proj_own", after=list(rope))
    near = _split_middle(near, after=[proj], name="allgather_w_in_near_forward")
    far, _ = _split_start(_gather_both_legs_rider(near[2][:1], near[2][1:], FAR_CHIPS), "allgather_w_in_far_start")
    _, bufs = _split_wait((near[0], near[1], far[2], near[3]), after=[], name="allgather_w_in_near_wait")
    proj = _proj_by_slot(h, blocked(bufs[0]), slot_order, 1, len(NEAR_CHIPS), "mm_proj_near", proj=proj)
    far = _split_middle((far[0], far[1], bufs, far[3]), after=[proj], name="allgather_w_in_far_forward")
    (w_in_buf, conv_w_buf, b_gate_buf), _ = _split_wait(far, after=[], name="allgather_w_in_far_wait")
    wts["w_in"] = w_in_buf.reshape(N_CHIPS, -1, w_in_buf.shape[3])
    wts["conv_w"] = conv_w_buf.transpose(1, 0, 2).reshape(CONV_WIDTH, -1)
    wts["b_gate"] = b_gate_buf.transpose(1, 0, 2).reshape(2, 1, -1)
    ch = wts["conv_w"].shape[1]
    gate_col0 = conv_col0 + 2 * ch
    n_mix = LATE_GATHER.index("w_ffn_in")
    mix_rider, ffn_rider = _gather_ici_rider(late_bufs[:n_mix], []), _gather_ici_rider(late_bufs[n_mix:], [])
    both, started = _split_start(_riders_together(mix_rider, ffn_rider), "late_gather_start", after=[wts["w_in"]])
    n_sem, n_buf = len(mix_rider.scratch), len(mix_rider.operands)
    mix_gather = (mix_rider, both[1][:n_sem], both[2][:n_buf], [])
    ffn_gather = (ffn_rider, both[1][n_sem:], both[2][n_buf:], [])
    proj = _proj_by_slot(h, wts["w_in"], slot_order, 1 + len(NEAR_CHIPS), len(FAR_CHIPS), "mm_proj_far", proj=proj,
                         after=[started])
    qn, kn = _qk_fwd(proj, rope, qw2, kw2, bd, "qk_fwd")
    attn, lse, attn_b = _attn_fwd(qn, kn, proj, "attn_fwd")

    def gathered(names, bufs):
        for n, buf in zip(names, bufs):
            full = buf.reshape(N_CHIPS, -1, buf.shape[3])
            wts[n] = full.reshape(-1, full.shape[2]) if n in ROW_SHARDED else full

    mix_bufs, _ = _split_wait(mix_gather, after=[attn_b], name="late_gather_mix_wait")
    (u0, uc), mix_bufs = _conv_fwd(proj, wts["conv_w"], wts["conv_b"], conv_col0, "conv_fwd",
                                   rider=_gather_forward_rider(mix_bufs))
    gathered(LATE_GATHER[:n_mix], mix_bufs)
    u3 = _ln_silu_fwd(uc, wts["conv_ln_w"], wts["conv_ln_b"], "ln_fwd")
    ffn_bufs, _ = _split_wait(ffn_gather, after=[u3], name="late_gather_ffn_wait")
    (y_a, y_b, mixed, x1, h2), ffn_bufs = _mix_out_proj(
        attn_b, u3, wts["w_o_attn"], wts["w_pw_conv"], proj, wts["b_gate"], gate_col0, wts["w_out"], x,
        wts["norm2_w"], "mix_x1_rms2", rider=_gather_forward_rider(ffn_bufs))
    gathered(LATE_GATHER[n_mix:], ffn_bufs)
    gate, up, act = _ffn_in_swiglu(h2, wts["w_ffn_in"], "mm_gu_swiglu")
    dy, dy_b16, loss = _ffn_out_loss(act, wts["w_ffn_out"], x1, target, "mm_x2_loss")

    g = {}
    by_chip = {}

    def pair_add(n, blocks, received):
        return _add_own_half(blocks, received, pos_arr, f"grads_pair_add_{n}")

    g_ffn_out = _blocks_by_half(
        _matmul(act, dy_b16, mode="tn", tm=1408, tn=1024, tk=2048, out_dtype=F32, name="mm_dwffnout"))
    dgu, (received,) = _ffn_out_bwd_swiglu(dy_b16, wts["w_ffn_out"], gate, up, "mm_dact_swiglu_bwd",
                                           rider=_pair_exchange_rider([g_ffn_out], halved=True))
    to_send, own = pair_add("w_ffn_out", g_ffn_out, received)
    (dx1, dx1_b16, g["norm2_w"]), (by_chip["w_ffn_out"],) = _matmul_nt_rmsnorm_bwd(
        dgu, wts["w_ffn_in"], x1, wts["norm2_w"], dy, "mm_dh2_rms2_bwd", rider=_chip_exchange_rider([to_send], [own]))
    g_ffn_in = _blocks_by_half(_matmul(h2, dgu, mode="tn", tm=512, tn=1408, tk=2048, out_dtype=F32,
                                       name="mm_dwffnin", out_blocked=N_CHIPS, cols_outer=True))
    exchanging, started = _split_start(_pair_exchange_rider([g_ffn_in], halved=True), "grads_ffn_in_pair_start")
    g["w_out"] = _matmul(mixed, dx1_b16, mode="tn", tm=512, tn=1024, tk=2048, out_dtype=F32, name="mm_dwout",
                         after=[started])
    dgl, dy_a, dy_b, g["b_gate"], dattn, du3 = _out_proj_bwd_gates(
        dx1_b16, wts["w_out"], proj, wts["b_gate"], y_a, y_b, wts["w_o_attn"], wts["w_pw_conv"], gate_col0,
        "mix_bwd")
    (received,), (g_ffn_in,) = _split_wait(exchanging, after=[dgl], name="grads_ffn_in_pair_wait")
    ffn_in_to_send, ffn_in_own = pair_add("w_ffn_in", g_ffn_in, received)
    g["w_o_attn"] = _matmul(attn_b, dy_a, mode="tn", tm=512, tn=256, tk=2048, out_dtype=F32, name="mm_dwo",
                            out_blocked=N_CHIPS)
    g["w_pw_conv"] = _matmul(u3, dy_b, mode="tn", tm=512, tn=256, tk=2048, out_dtype=F32, name="mm_dwpw",
                             out_blocked=N_CHIPS)
    duc, g["conv_ln_w"], g["conv_ln_b"] = _ln_silu_bwd(du3, uc, wts["conv_ln_w"], wts["conv_ln_b"], "ln_bwd")

    small3 = ("w_out", "w_o_attn", "w_pw_conv")
    g_small3 = [_blocks_by_half(g.pop(n)) for n in small3]
    (da, db, g["conv_w"], g["conv_b"]), received = _conv_bwd(
        duc, u0, proj, wts["conv_w"], conv_col0, "conv_bwd", rider=_pair_exchange_rider(g_small3, halved=True))
    sums3 = [pair_add(n, gb, rv) for n, gb, rv in zip(small3, g_small3, received)]
    (dqn, dkn, dv), (by_chip["w_ffn_in"],) = _attn_bwd(
        qn, kn, proj, dattn, attn, lse, bd, "attn_bwd",
        rider=_chip_exchange_rider([ffn_in_to_send], [ffn_in_own]))
    (dproj, dqw, dkw), exchanged3 = _qk_bwd(
        dqn, dkn, dv, da, db, dgl, proj, rope, qw2, kw2, bd, "qk_bwd",
        rider=_chip_exchange_rider([s[0] for s in sums3], [s[1] for s in sums3]))
    by_chip.update(zip(small3, exchanged3))
    halves = [_sum_chips(by_chip[n], pos_arr, f"grads_chip_sum_{n}") for n in EARLY_REDUCE]
    g["q_norm_w"] = dqw[:, :HEAD_DIM]
    g["k_norm_w"] = dkw[:, :HEAD_DIM]

    c = pos_arr[0]
    rh = h.shape[1] // 2
    h_sibling = lax.dynamic_slice_in_dim(h, (1 - c) * rh, rh, axis=1)
    h_own = lax.dynamic_slice_in_dim(h, c * rh, rh, axis=1)
    g_sibling, shards = _matmul(h_sibling, dproj, mode="tn", tm=rh, tn=1920, tk=2048, out_dtype=F32,
                                name="mm_dwin_sibling", out_blocked=N_CHIPS, rider=_pair_gather_rider(halves))
    reduced = dict(zip(EARLY_REDUCE, shards))
    exchanging, started = _split_start(_pair_exchange_rider([g_sibling], halved=False), "grads_w_in_pair_start")
    g_own = _matmul(h_own, dproj, mode="tn", tm=rh, tn=1920, tk=2048, out_dtype=F32, name="mm_dwin_own",
                    out_blocked=N_CHIPS, after=[started])
    (from_sibling,), _ = _split_wait(exchanging, after=[g_own], name="grads_w_in_pair_wait")
    to_send, own = _add_own_half(g_own, from_sibling, pos_arr, "grads_pair_add_w_in")
    in_flight, started = _split_start(_chip_exchange_rider([to_send], [own]), "grads_w_in_exchange_start")
    grad_x, _, g["norm1_w"] = _matmul_nt_rmsnorm_bwd(dproj, wts["w_in"], x, wts["norm1_w"], dx1, "mm_dh_rms1_bwd",
                                                     after=[started])
    return loss, grad_x, g, reduced, (in_flight, started)


def _mesh_pos():
    return lax.axis_index("x"), lax.axis_index("y"), lax.axis_index("c")


def _other_chips(x, y):
    return [(1 - x, y), (x, 1 - y), (1 - x, 1 - y)]


NEAR_CHIPS, FAR_CHIPS = (0, 1), (2,)


def _cast_into_slot(shard, chip_arr, dtype, name, n_slots=N_CHIPS, after=()):
    r, c = shard.shape
    tr = r // 2 if r % 32 == 0 else r

    def body(chip_ref, s_ref, *refs):
        refs[-1][...] = s_ref[...].astype(dtype)

    return pl.pallas_call(
        body, name=name,
        grid_spec=pltpu.PrefetchScalarGridSpec(
            num_scalar_prefetch=1, grid=(r // tr,),
            in_specs=[pl.BlockSpec((tr, c), lambda i, chip_ref: (i, 0))] + [ANY] * len(after),
            out_specs=pl.BlockSpec((None, tr, c), lambda i, chip_ref: (chip_ref[0], i, 0))),
        out_shape=jax.ShapeDtypeStruct((n_slots, r, c), dtype), compiler_params=_params(),
    )(chip_arr, shard, *after)


GATHER_CHUNKS = 4


def _gather_both_legs_rider(big, small, peers):
    nb = len(big)
    n = nb + len(small)
    nch = GATHER_CHUNKS

    def part(bufs, a, slot, half, ch):
        if a >= nb:
            return bufs[a].at[slot]
        rows = bufs[a].shape[2] // nch
        return bufs[a].at[slot, half, pl.ds(ch * rows, rows)]

    def pieces():
        return [(a, ch, k) for ch in range(nch) for a in range(n) for k in peers if a < nb or ch == 0]

    def ici(bufs, sems, a, ch, k, slot_of_src):
        x, y, c = _mesh_pos()
        px, py = _other_chips(x, y)[k]
        slot = 2 * x + y if slot_of_src == "mine" else 2 * px + py
        return pltpu.make_async_remote_copy(
            src_ref=part(bufs, a, slot, c, ch), dst_ref=part(bufs, a, slot, c, ch), send_sem=sems[0].at[a, ch, k],
            recv_sem=sems[1].at[a, ch, k], device_id=(px, py, c), device_id_type=MESH)

    def forward(bufs, sems, a, ch, k, half):
        x, y, c = _mesh_pos()
        px, py = _other_chips(x, y)[k]
        h = c if half == "mine" else 1 - c
        return pltpu.make_async_remote_copy(
            src_ref=part(bufs, a, 2 * px + py, h, ch), dst_ref=part(bufs, a, 2 * px + py, h, ch),
            send_sem=sems[2].at[a, ch, k], recv_sem=sems[3].at[a, ch, k], device_id=(x, y, 1 - c),
            device_id_type=MESH)

    def start(r_in, bufs, sems):
        for a, ch, k in pieces():
            ici(bufs, sems, a, ch, k, "mine").start()

    def middle(r_in, bufs, sems):
        for a, ch, k in pieces():
            ici(bufs, sems, a, ch, k, "theirs").wait_recv()
            if a < nb:
                forward(bufs, sems, a, ch, k, "mine").start()
        for a, ch, k in pieces():
            ici(bufs, sems, a, ch, k, "mine").wait_send()

    def wait(r_in, bufs, sems):
        for a, ch, k in pieces():
            if a < nb:
                forward(bufs, sems, a, ch, k, "theirs").wait_recv()
        for a, ch, k in pieces():
            if a < nb:
                forward(bufs, sems, a, ch, k, "mine").wait_send()

    ops = list(big) + list(small)
    return _Rider(ops, [jax.ShapeDtypeStruct(o.shape, o.dtype) for o in ops], {i: i for i in range(n)},
                  [pltpu.SemaphoreType.DMA((n, nch, 3)), pltpu.SemaphoreType.DMA((n, nch, 3)),
                   pltpu.SemaphoreType.DMA((nb, nch, 3)), pltpu.SemaphoreType.DMA((nb, nch, 3))],
                  start, wait, middle)


def _comm_call(rider, name):
    def body():
        pass

    return _pallas(body, name=name, grid=(1,), in_specs=[], out_specs=[], out_shape=[], operands=[],
                   rider=rider)[1]


def _gather_ici_rider(big, small):
    nb = len(big)
    n = nb + len(small)

    def copies(bufs, sems):
        x, y, c = _mesh_pos()
        me = 2 * x + y
        part = lambda a, slot: bufs[a].at[slot, c] if a < nb else bufs[a].at[slot]
        out = []
        for a in range(n):
            for k, (px, py) in enumerate(_other_chips(x, y)):
                send = functools.partial(
                    pltpu.make_async_remote_copy,
                    src_ref=part(a, me), dst_ref=part(a, me), send_sem=sems[0].at[a, k],
                    recv_sem=sems[1].at[a, k], device_id=(px, py, c), device_id_type=MESH)
                recv = functools.partial(
                    pltpu.make_async_remote_copy,
                    src_ref=part(a, 2 * px + py), dst_ref=part(a, 2 * px + py), send_sem=sems[0].at[a, k],
                    recv_sem=sems[1].at[a, k], device_id=(px, py, c), device_id_type=MESH)
                out.append((send, recv))
        return out

    def start(r_in, r_out, sems):
        for send, _ in copies(r_out, sems):
            send().start()

    def wait(r_in, r_out, sems):
        cps = copies(r_out, sems)
        for _, recv in cps:
            recv().wait_recv()
        for send, _ in cps:
            send().wait_send()

    ops = list(big) + list(small)
    return _Rider(ops, [jax.ShapeDtypeStruct(o.shape, o.dtype) for o in ops], {i: i for i in range(n)},
                  [pltpu.SemaphoreType.DMA((n, 3)), pltpu.SemaphoreType.DMA((n, 3))], start, wait)


def _gather_forward_rider(big):
    n = len(big)

    def copies(bufs, sems):
        x, y, c = _mesh_pos()
        out = []
        for a in range(n):
            for k, (px, py) in enumerate(_other_chips(x, y)):
                slot = 2 * px + py
                send = functools.partial(
                    pltpu.make_async_remote_copy,
                    src_ref=bufs[a].at[slot, c], dst_ref=bufs[a].at[slot, c], send_sem=sems[0].at[a, k],
                    recv_sem=sems[1].at[a, k], device_id=(x, y, 1 - c), device_id_type=MESH)
                recv = functools.partial(
                    pltpu.make_async_remote_copy,
                    src_ref=bufs[a].at[slot, 1 - c], dst_ref=bufs[a].at[slot, 1 - c], send_sem=sems[0].at[a, k],
                    recv_sem=sems[1].at[a, k], device_id=(x, y, 1 - c), device_id_type=MESH)
                out.append((send, recv))
        return out

    def start(r_in, r_out, sems):
        for send, _ in copies(r_out, sems):
            send().start()

    def wait(r_in, r_out, sems):
        cps = copies(r_out, sems)
        for _, recv in cps:
            recv().wait_recv()
        for send, _ in cps:
            send().wait_send()

    return _Rider(big, [jax.ShapeDtypeStruct(o.shape, o.dtype) for o in big], {i: i for i in range(n)},
                  [pltpu.SemaphoreType.DMA((n, 3)), pltpu.SemaphoreType.DMA((n, 3))], start, wait)


def _pair_exchange_rider(gs, halved):
    n = len(gs)

    def copies(r_in, r_out, sems):
        x, y, c = _mesh_pos()
        return [pltpu.make_async_remote_copy(
            src_ref=r_in[a].at[:, 1 - c] if halved else r_in[a], dst_ref=r_out[a], send_sem=sems[0].at[a],
            recv_sem=sems[1].at[a], device_id=(x, y, 1 - c), device_id_type=MESH) for a in range(n)]

    def start(r_in, r_out, sems):
        for cp in copies(r_in, r_out, sems):
            cp.start()

    def wait(r_in, r_out, sems):
        for cp in copies(r_in, r_out, sems):
            cp.wait()

    return _Rider(gs, [jax.ShapeDtypeStruct((g.shape[0],) + g.shape[-2:], g.dtype) for g in gs], {},
                  [pltpu.SemaphoreType.DMA((n,)), pltpu.SemaphoreType.DMA((n,))], start, wait)


def _chip_exchange_rider(to_send, by_chip, row_range=None):
    n = len(to_send)

    def copies(r_in, r_out, sems):
        x, y, c = _mesh_pos()
        me = 2 * x + y
        rows = (lambda ref: ref) if row_range is None else (lambda ref: ref.at[pl.ds(*row_range)])
        out = []
        for a in range(n):
            for k, (px, py) in enumerate(_other_chips(x, y)):
                send = functools.partial(
                    pltpu.make_async_remote_copy,
                    src_ref=rows(r_in[a].at[2 * px + py]), dst_ref=rows(r_out[a].at[me]),
                    send_sem=sems[0].at[a, k], recv_sem=sems[1].at[a, k], device_id=(px, py, c),
                    device_id_type=MESH)
                recv = functools.partial(
                    pltpu.make_async_remote_copy,
                    src_ref=rows(r_in[a].at[me]), dst_ref=rows(r_out[a].at[2 * px + py]),
                    send_sem=sems[0].at[a, k], recv_sem=sems[1].at[a, k], device_id=(px, py, c),
                    device_id_type=MESH)
                out.append((send, recv))
        return out

    def start(r_in, r_out, sems):
        for send, _ in copies(r_in, r_out, sems):
            send().start()

    def wait(r_in, r_out, sems):
        cps = copies(r_in, r_out, sems)
        for _, recv in cps:
            recv().wait_recv()
        for send, _ in cps:
            send().wait_send()

    return _Rider(list(to_send) + list(by_chip), [jax.ShapeDtypeStruct(b.shape, b.dtype) for b in by_chip],
                  {n + i: i for i in range(n)},
                  [pltpu.SemaphoreType.DMA((n, 3)), pltpu.SemaphoreType.DMA((n, 3))], start, wait)


HBM = pl.BlockSpec(memory_space=pltpu.HBM)
SEM = pl.BlockSpec(memory_space=pltpu.SEMAPHORE)


_IN_FLIGHT = pltpu.CompilerParams(has_side_effects=pltpu.SideEffectType.DATAFLOW_SIDE_EFFECTING)


class _FlatSems:
    def __init__(self, ref, shape):
        self.ref, self.shape = ref, shape

    @property
    def at(self):
        return self

    def __getitem__(self, idx):
        idx = idx if isinstance(idx, tuple) else (idx,)
        flat = 0
        for i, n in zip(idx, self.shape):
            flat = flat * n + i
        return self.ref.at[flat]


def _flat_sem_types(rider):
    return tuple(pltpu.SemaphoreType.DMA((int(np.prod(s.shape)),)) for s in rider.scratch)


def _as_rider_sems(rider, refs):
    return [_FlatSems(r, s.shape) for r, s in zip(refs, rider.scratch)]


def _split_start(rider, name, after=()):
    n_in, n_out, n_sem = len(rider.operands), len(rider.out_shapes), len(rider.scratch)
    n_after = len(after)
    fresh = [j for j in range(n_out) if j not in rider.aliases.values()]
    by_out = {j: i for i, j in rider.aliases.items()}

    def body(*refs):
        r_in = refs[:n_in]
        refs = refs[n_in + n_after:]
        sems = refs[:n_sem]
        thru = refs[n_sem:n_sem + n_in]
        fresh_refs = refs[n_sem + n_in:n_sem + n_in + len(fresh)]
        token = refs[-1]
        r_out = [thru[by_out[j]] if j in by_out else fresh_refs[fresh.index(j)] for j in range(n_out)]
        rider.start(r_in, r_out, _as_rider_sems(rider, sems))
        token[...] = jnp.zeros_like(token)

    res = pl.pallas_call(
        body, name=name,
        out_shape=_flat_sem_types(rider) + tuple(pltpu.HBM(o.shape, o.dtype) for o in rider.operands)
        + tuple(pltpu.HBM(rider.out_shapes[j].shape, rider.out_shapes[j].dtype) for j in fresh)
        + (jax.ShapeDtypeStruct((8, LANES), F32),),
        in_specs=(HBM,) * n_in + (ANY,) * n_after,
        out_specs=(SEM,) * n_sem + (HBM,) * (n_in + len(fresh)) + (pl.BlockSpec(memory_space=pltpu.VMEM),),
        input_output_aliases={i: n_sem + i for i in range(n_in)}, compiler_params=_IN_FLIGHT,
    )(*[pltpu.with_memory_space_constraint(o, pltpu.HBM) for o in rider.operands], *after)
    return (rider, res[:n_sem], res[n_sem:n_sem + n_in], res[n_sem + n_in:-1]), res[-1]


def _split_continue(handles, after, name, phase):
    rider, sems, thru, fresh_arrays = handles
    n_in, n_out, n_sem = len(rider.operands), len(rider.out_shapes), len(rider.scratch)
    fresh = [j for j in range(n_out) if j not in rider.aliases.values()]
    by_out = {j: i for i, j in rider.aliases.items()}
    n_data = n_in + len(fresh)

    def body(*refs):
        r_in = refs[:n_in]
        fresh_refs = refs[n_in:n_data]
        sem_refs = refs[n_data:n_data + n_sem]
        r_out = [r_in[by_out[j]] if j in by_out else fresh_refs[fresh.index(j)] for j in range(n_out)]
        phase(r_in, r_out, _as_rider_sems(rider, sem_refs))

    data = list(thru) + list(fresh_arrays)
    return pl.pallas_call(
        body, name=name, out_shape=tuple(pltpu.HBM(d.shape, d.dtype) for d in data),
        in_specs=(HBM,) * n_data + (SEM,) * n_sem + (ANY,) * len(after), out_specs=(HBM,) * n_data,
        input_output_aliases={i: i for i in range(n_data)}, compiler_params=_IN_FLIGHT,
    )(*data, *sems, *after)


def _split_middle(handles, after, name):
    rider, sems, thru, _ = handles
    res = _split_continue(handles, after, name, rider.middle)
    return rider, sems, res[:len(thru)], res[len(thru):]


def _split_wait(handles, after, name):
    rider = handles[0]
    n_in, n_out = len(rider.operands), len(rider.out_shapes)
    fresh = [j for j in range(n_out) if j not in rider.aliases.values()]
    by_out = {j: i for i, j in rider.aliases.items()}
    res = _split_continue(handles, after, name, rider.wait)
    return [res[by_out[j]] if j in by_out else res[n_in + fresh.index(j)] for j in range(n_out)], res[:n_in]


def _pair_gather_rider(bufs):
    n = len(bufs)

    def copies(r_out, sems):
        x, y, c = _mesh_pos()
        out = []
        for a in range(n):
            send = functools.partial(
                    pltpu.make_async_remote_copy,
                src_ref=r_out[a].at[c], dst_ref=r_out[a].at[c], send_sem=sems[0].at[a],
                recv_sem=sems[1].at[a], device_id=(x, y, 1 - c), device_id_type=MESH)
            recv = functools.partial(
                    pltpu.make_async_remote_copy,
                src_ref=r_out[a].at[1 - c], dst_ref=r_out[a].at[1 - c], send_sem=sems[0].at[a],
                recv_sem=sems[1].at[a], device_id=(x, y, 1 - c), device_id_type=MESH)
            out.append((send, recv))
        return out

    def start(r_in, r_out, sems):
        for send, _ in copies(r_out, sems):
            send().start()

    def wait(r_in, r_out, sems):
        cps = copies(r_out, sems)
        for _, recv in cps:
            recv().wait_recv()
        for send, _ in cps:
            send().wait_send()

    return _Rider(bufs, [jax.ShapeDtypeStruct(b.shape, b.dtype) for b in bufs], {i: i for i in range(n)},
                  [pltpu.SemaphoreType.DMA((n,)), pltpu.SemaphoreType.DMA((n,))], start, wait)


def _add_own_half(g, recv, pos_arr, name):
    nb, rh, cols = g.shape[0], g.shape[-2], g.shape[-1]

    def body(pos_ref, g_ref, r_ref, send_ref, own_ref):
        s = (g_ref[...] + r_ref[...]).astype(BF16)
        send_ref[...] = s

        @pl.when(pl.program_id(0) == pos_ref[1])
        def _():
            own_ref[...] = s

    blk = pl.BlockSpec((None, rh, cols), lambda j, pos_ref: (j, 0, 0))
    g_spec = blk if g.ndim == 3 else pl.BlockSpec((None, None, rh, cols),
                                                   lambda j, pos_ref: (j, pos_ref[0], 0, 0))
    shape = jax.ShapeDtypeStruct((nb, rh, cols), BF16)
    return pl.pallas_call(
        body, name=name,
        grid_spec=pltpu.PrefetchScalarGridSpec(
            num_scalar_prefetch=1, grid=(nb,), in_specs=[g_spec, blk],
            out_specs=[blk, pl.BlockSpec((None, rh, cols), lambda j, pos_ref: (pos_ref[1], 0, 0))]),
        out_shape=[shape, shape], compiler_params=_params(),
    )(pos_arr, g, recv)


def _sum_chips(gath, pos_arr, name):
    nb, rh, cols = gath.shape

    def body(pos_ref, a_ref, b_ref, c_ref, d_ref, o_ref):
        del pos_ref
        o_ref[...] = ((a_ref[...].astype(F32) + b_ref[...].astype(F32)) + c_ref[...].astype(F32)) \
            + d_ref[...].astype(F32)

    tr = rh // 2 if (rh // 2) % 16 == 0 else rh
    specs = [pl.BlockSpec((None, tr, cols), functools.partial(lambda i, pos_ref, j: (j, i, 0), j=j))
             for j in range(nb)]
    return pl.pallas_call(
        body, name=name,
        grid_spec=pltpu.PrefetchScalarGridSpec(
            num_scalar_prefetch=1, grid=(rh // tr,), in_specs=specs,
            out_specs=pl.BlockSpec((None, tr, cols), lambda i, pos_ref: (pos_ref[0], i, 0))),
        out_shape=jax.ShapeDtypeStruct((2, rh, cols), F32), compiler_params=_params(),
    )(pos_arr, gath, gath, gath, gath)


N_DEVICES = 8


def _small_gather_rider(buf):
    def copies(r_out, sems):
        x, y, c = _mesh_pos()
        me = 4 * x + 2 * y + c
        out = []
        for r in range(1, N_DEVICES):
            px = 1 - x if r & 4 else x
            py = 1 - y if r & 2 else y
            pc = 1 - c if r & 1 else c
            out.append(pltpu.make_async_remote_copy(
                src_ref=r_out[0].at[me], dst_ref=r_out[0].at[me], send_sem=sems[0].at[r - 1],
                recv_sem=sems[1].at[r - 1], device_id=(px, py, pc), device_id_type=MESH))
        return out

    def start(r_in, r_out, sems):
        for cp in copies(r_out, sems):
            cp.start()

    def wait(r_in, r_out, sems):
        cps = copies(r_out, sems)
        for cp in cps:
            cp.wait_recv()
        for cp in cps:
            cp.wait_send()

    return _Rider([buf], [jax.ShapeDtypeStruct(buf.shape, buf.dtype)], {0: 0},
                  [pltpu.SemaphoreType.DMA((N_DEVICES - 1,)), pltpu.SemaphoreType.DMA((N_DEVICES - 1,))],
                  start, wait)


def _sum_devices(buf, name):
    def body(b_ref, o_ref):
        acc = b_ref[0]
        for i in range(1, N_DEVICES):
            acc = acc + b_ref[i]
        o_ref[...] = acc

    return _pallas(body, name=name, grid=(1,), in_specs=[pl.BlockSpec(buf.shape, lambda i: (0, 0, 0))],
                   out_specs=pl.BlockSpec(buf.shape[1:], lambda i: (0, 0)),
                   out_shape=jax.ShapeDtypeStruct(buf.shape[1:], F32), operands=[buf])


def _adamw_math(w, g, m, v):
    m = ADAM_B1 * m + (1.0 - ADAM_B1) * g
    v = ADAM_B2 * v + (1.0 - ADAM_B2) * (g * g)
    m_hat = m / (1.0 - ADAM_B1 ** ADAM_STEP)
    v_hat = v / (1.0 - ADAM_B2 ** ADAM_STEP)
    delta = -ADAM_LR * (m_hat / (jnp.sqrt(v_hat) + ADAM_EPS) + ADAM_WD * w)
    return delta, m, v


def _adamw(w, g, m, v, name, after=()):
    r, c = w.shape
    tr = next(t for t in (256, 352, 128, 64) if r % t == 0 and r >= 2 * t)

    def body(w_ref, g_ref, m_ref, v_ref, go_ref, d_ref, mo_ref, vo_ref):
        gv = g_ref[...]
        d, mn, vn = _adamw_math(w_ref[...], gv, m_ref[...], v_ref[...])
        go_ref[...] = gv
        d_ref[...] = d
        mo_ref[...] = mn
        vo_ref[...] = vn

    blk = pl.BlockSpec((tr, c), lambda i: (i, 0))
    return _pallas(body, name=name, grid=(r // tr,), in_specs=[blk] * 4, out_specs=[blk] * 4,
                   out_shape=[jax.ShapeDtypeStruct((r, c), F32)] * 4, operands=[w, g, m, v], after=after)


def _adamw_small(ws, gs, ms, vs, name):
    n = len(ws)

    def body(*refs):
        w_r, g_r, m_r, v_r = refs[:n], refs[n:2 * n], refs[2 * n:3 * n], refs[3 * n:4 * n]
        d_o, m_o, v_o = refs[4 * n:5 * n], refs[5 * n:6 * n], refs[6 * n:7 * n]
        for i in range(n):
            d, mn, vn = _adamw_math(w_r[i][...], g_r[i][...], m_r[i][...], v_r[i][...])
            d_o[i][...] = d
            m_o[i][...] = mn
            v_o[i][...] = vn

    specs = [pl.BlockSpec(w.shape, lambda i: (0, 0)) for w in ws]
    shapes = [jax.ShapeDtypeStruct(w.shape, F32) for w in ws]
    outs = pl.pallas_call(
        body, name=name, grid=(1,), in_specs=specs * 4, out_specs=specs * 3, out_shape=shapes * 3,
        compiler_params=_params(),
    )(*ws, *gs, *ms, *vs)
    return outs[:n], outs[n:2 * n], outs[2 * n:]


BIG = ("w_in", "w_o_attn", "w_pw_conv", "w_out", "w_ffn_in", "w_ffn_out")
ROW_SHARDED = ("w_out", "w_ffn_out")
SMALL = ("norm1_w", "b_gate", "q_norm_w", "k_norm_w", "conv_w", "conv_b", "conv_ln_w", "conv_ln_b", "norm2_w")
ORDER = ("norm1_w", "w_in", "b_gate", "q_norm_w", "k_norm_w", "w_o_attn", "conv_w", "conv_b", "conv_ln_w",
         "conv_ln_b", "w_pw_conv", "w_out", "norm2_w", "w_ffn_in", "w_ffn_out")
PACK_TILE = 8 * LANES


def _pack_small(parts):
    rows = []
    for p in parts:
        flat = p.reshape(-1)
        pad = (-flat.shape[0]) % PACK_TILE
        rows.append(jnp.pad(flat, (0, pad)).reshape(-1, LANES))
    return jnp.concatenate(rows, axis=0)


def _unpack_small(packed, shapes):
    out, row = [], 0
    for shp in shapes:
        size = int(np.prod(shp))
        nrow = -(-size // PACK_TILE) * (PACK_TILE // LANES)
        out.append(packed[row:row + nrow].reshape(-1)[:size].reshape(shp))
        row += nrow
    return out


def kernel(x, positions, norm1_w, w_in, b_gate, q_norm_w, k_norm_w, w_o_attn, conv_w, conv_b, conv_ln_w, conv_ln_b, w_pw_conv, w_out, norm2_w, w_ffn_in, w_ffn_out, loss_target, m_norm1_w, m_w_in, m_b_gate, m_q_norm_w, m_k_norm_w, m_w_o_attn, m_conv_w, m_conv_b, m_conv_ln_w, m_conv_ln_b, m_w_pw_conv, m_w_out, m_norm2_w, m_w_ffn_in, m_w_ffn_out, v_norm1_w, v_w_in, v_b_gate, v_q_norm_w, v_k_norm_w, v_w_o_attn, v_conv_w, v_conv_b, v_conv_ln_w, v_conv_ln_b, v_w_pw_conv, v_w_out, v_norm2_w, v_w_ffn_in, v_w_ffn_out):
    w = dict(norm1_w=norm1_w, w_in=w_in, b_gate=b_gate, q_norm_w=q_norm_w, k_norm_w=k_norm_w, w_o_attn=w_o_attn,
             conv_w=conv_w, conv_b=conv_b, conv_ln_w=conv_ln_w, conv_ln_b=conv_ln_b, w_pw_conv=w_pw_conv,
             w_out=w_out, norm2_w=norm2_w, w_ffn_in=w_ffn_in, w_ffn_out=w_ffn_out)
    m = dict(norm1_w=m_norm1_w, w_in=m_w_in, b_gate=m_b_gate, q_norm_w=m_q_norm_w, k_norm_w=m_k_norm_w,
             w_o_attn=m_w_o_attn, conv_w=m_conv_w, conv_b=m_conv_b, conv_ln_w=m_conv_ln_w,
             conv_ln_b=m_conv_ln_b, w_pw_conv=m_w_pw_conv, w_out=m_w_out, norm2_w=m_norm2_w,
             w_ffn_in=m_w_ffn_in, w_ffn_out=m_w_ffn_out)
    v = dict(norm1_w=v_norm1_w, w_in=v_w_in, b_gate=v_b_gate, q_norm_w=v_q_norm_w, k_norm_w=v_k_norm_w,
             w_o_attn=v_w_o_attn, conv_w=v_conv_w, conv_b=v_conv_b, conv_ln_w=v_conv_ln_w,
             conv_ln_b=v_conv_ln_b, w_pw_conv=v_w_pw_conv, w_out=v_w_out, norm2_w=v_norm2_w,
             w_ffn_in=v_w_ffn_in, w_ffn_out=v_w_ffn_out)
    cx, cy, cc = _mesh_pos()
    chip = 2 * cx + cy

    chip_arr = chip.reshape(1).astype(jnp.int32)
    pos_arr = jnp.stack([cc, chip]).astype(jnp.int32)
    halves = lambda buf: buf.reshape(N_CHIPS, 2, buf.shape[1] // 2, buf.shape[2])
    w_in_buf = halves(_cast_into_slot(w["w_in"][0], chip_arr, BF16, "cast_w_in"))
    small_bufs = [_cast_into_slot(w[n][0], chip_arr, F32, f"slot_{n}") for n in ("conv_w", "b_gate")]
    first_gather, started = _split_start(_gather_both_legs_rider([w_in_buf], small_bufs, NEAR_CHIPS),
                                         "allgather_w_in_near_start")
    late_bufs = [halves(_cast_into_slot(w[n][0], chip_arr, BF16, f"cast_{n}", after=[started]))
                 for n in LATE_GATHER]
    wts = dict(norm1_w=norm1_w, q_norm_w=q_norm_w, k_norm_w=k_norm_w, conv_b=conv_b, conv_ln_w=conv_ln_w,
               conv_ln_b=conv_ln_b, norm2_w=norm2_w)

    loss, grad_x, g, reduced, w_in_in_flight = _forward_backward(
        x[0], positions.reshape(-1, 1), loss_target[0], wts, first_gather, late_bufs, pos_arr)
    grads = {n: b.reshape(-1, b.shape[2]) for n, b in reduced.items()}

    w_in_in_flight, started = w_in_in_flight
    delta, new_m, new_v = {}, {}, {}
    for n in EARLY_REDUCE:
        grads[n], delta[n], new_m[n], new_v[n] = _adamw(w[n][0], grads[n], m[n][0], v[n][0], f"adamw_{n}",
                                                        after=[started])
    small_parts = [loss] + [g[n] for n in SMALL]
    small_shapes = [p.shape for p in small_parts]
    device_arr = (4 * cx + 2 * cy + cc).reshape(1).astype(jnp.int32)
    small_buf = _cast_into_slot(_pack_small(small_parts), device_arr, F32, "slot_small", n_slots=N_DEVICES)

    (by_chip_w_in,), _ = _split_wait(w_in_in_flight, after=[delta[n] for n in EARLY_REDUCE] + [small_buf],
                                     name="grads_w_in_exchange_wait")
    half_w_in = _sum_chips(by_chip_w_in, pos_arr, "grads_chip_sum_w_in")
    shard_w_in, small_buf = _comm_call(
        _riders_together(_pair_gather_rider([half_w_in]), _small_gather_rider(small_buf)),
        "grads_pair_gather_w_in_small_gather")
    summed = _sum_devices(small_buf, "small_sum")
    reduced = _unpack_small(summed, small_shapes)
    loss_total = reduced[0].reshape(())
    for n, r in zip(SMALL, reduced[1:]):
        grads[n] = r
    ch_shard = conv_w.shape[2]
    grads["conv_w"] = lax.dynamic_slice_in_dim(grads["conv_w"], chip * ch_shard, ch_shard, axis=1)
    d_shard = b_gate.shape[2]
    grads["b_gate"] = lax.dynamic_slice_in_dim(grads["b_gate"], chip * d_shard, d_shard, axis=1)

    grads["w_in"], delta["w_in"], new_m["w_in"], new_v["w_in"] = _adamw(
        w["w_in"][0], shard_w_in.reshape(-1, shard_w_in.shape[2]), m["w_in"][0], v["w_in"][0], "adamw_w_in")
    flat2 = lambda a: a.reshape(-1, a.shape[-1])
    d_s, m_s, v_s = _adamw_small([flat2(w[n]) for n in SMALL], [flat2(grads[n]) for n in SMALL],
                                 [flat2(m[n]) for n in SMALL], [flat2(v[n]) for n in SMALL], "adamw_small")
    for i, n in enumerate(SMALL):
        delta[n], new_m[n], new_v[n] = d_s[i], m_s[i], v_s[i]

    shaped = lambda d, n: d[n].reshape(w[n].shape)
    return (loss_total, grad_x[None], *[shaped(grads, n) for n in ORDER], *[shaped(delta, n) for n in ORDER],
            *[shaped(new_m, n) for n in ORDER], *[shaped(new_v, n) for n in ORDER])
```

```python
import functools

import numpy as np
import jax
import jax.numpy as jnp
from jax import lax
from jax.experimental import pallas as pl
from jax.experimental.pallas import tpu as pltpu

F32 = jnp.float32
BF16 = jnp.bfloat16
MESH = pl.DeviceIdType.MESH
ANY = pl.BlockSpec(memory_space=pl.ANY)

HEAD_DIM = 64
N_SLOT_HEADS = 8
DILATIONS = (1, 4, 16)
HALF_SPAN = 64
ROPE_THETA = 500000.0
ROT_DIM = 16
CONV_WIDTH = 31
EPS = 1e-6
NEG_INF = -1e30
ADAM_LR, ADAM_B1, ADAM_B2, ADAM_EPS, ADAM_WD, ADAM_STEP = 0.001, 0.9, 0.999, 1e-08, 0.01, 10

LANES = 128
QBLK = 128
KWIN = QBLK + 2 * HALF_SPAN
VMEM_LIMIT = 48 * 1024 * 1024
N_CHIPS = 4


def _params(**kw):
    return pltpu.CompilerParams(vmem_limit_bytes=VMEM_LIMIT, **kw)


class _Rider:
    def __init__(self, operands, out_shapes, aliases, scratch, start, wait, middle=None):
        self.operands, self.out_shapes, self.aliases = list(operands), list(out_shapes), dict(aliases)
        self.scratch, self.start, self.wait = list(scratch), start, wait
        self.middle = middle


def _riders_together(a, b):
    n_in, n_out, n_sc = len(a.operands), len(a.out_shapes), len(a.scratch)
    aliases = dict(a.aliases)
    aliases.update({n_in + src: n_out + dst for src, dst in b.aliases.items()})

    def start(r_in, r_out, r_sc):
        a.start(r_in[:n_in], r_out[:n_out], r_sc[:n_sc])
        b.start(r_in[n_in:], r_out[n_out:], r_sc[n_sc:])

    def wait(r_in, r_out, r_sc):
        a.wait(r_in[:n_in], r_out[:n_out], r_sc[:n_sc])
        b.wait(r_in[n_in:], r_out[n_out:], r_sc[n_sc:])

    return _Rider(a.operands + b.operands, a.out_shapes + b.out_shapes, aliases, a.scratch + b.scratch, start, wait)


def _pallas(body, *, name, grid, in_specs, out_specs, out_shape, operands, scratch_shapes=(), aliases=None,
            rider=None, after=()):
    single = not isinstance(out_specs, (list, tuple))
    out_specs_l = [out_specs] if single else list(out_specs)
    out_shape_l = [out_shape] if single else list(out_shape)
    aliases = dict(aliases or {})

    def call(fn, all_in_specs, all_out_specs, all_out_shape, all_scratch, all_aliases, all_operands):
        return pl.pallas_call(
            fn, name=name, grid=grid, in_specs=all_in_specs, out_specs=all_out_specs, out_shape=all_out_shape,
            scratch_shapes=all_scratch, input_output_aliases=all_aliases, compiler_params=_params(),
        )(*all_operands)

    if rider is None:
        n_main = len(in_specs)

        def ordered(*refs):
            body(*refs[:n_main], *refs[n_main + len(after):])

        res = call(ordered if after else body, list(in_specs) + [ANY] * len(after), out_specs_l, out_shape_l,
                   list(scratch_shapes), aliases, list(operands) + list(after))
        return res[0] if single else res
    assert not after
    n_in, n_rin = len(in_specs), len(rider.operands)
    n_out, n_rout = len(out_specs_l), len(rider.out_shapes)
    n_sc = len(scratch_shapes)

    def wrapped(*refs):
        main_in, r_in = refs[:n_in], refs[n_in:n_in + n_rin]
        o0 = n_in + n_rin
        main_out, r_out = refs[o0:o0 + n_out], refs[o0 + n_out:o0 + n_out + n_rout]
        s0 = o0 + n_out + n_rout
        main_sc, r_sc = refs[s0:s0 + n_sc], refs[s0 + n_sc:]
        ids = [pl.program_id(d) for d in range(len(grid))]
        first = functools.reduce(jnp.logical_and, [i == 0 for i in ids])
        last = functools.reduce(jnp.logical_and, [i == n - 1 for i, n in zip(ids, grid)])

        @pl.when(first)
        def _():
            rider.start(r_in, r_out, r_sc)

        body(*main_in, *main_out, *main_sc)

        @pl.when(last)
        def _():
            rider.wait(r_in, r_out, r_sc)

    for src, dst in rider.aliases.items():
        aliases[n_in + src] = n_out + dst
    res = call(wrapped, list(in_specs) + [ANY] * n_rin, out_specs_l + [ANY] * n_rout,
               out_shape_l + rider.out_shapes, list(scratch_shapes) + rider.scratch, aliases,
               list(operands) + rider.operands)
    main = res[:n_out]
    return (main[0] if single else main), res[n_out:]


def _matmul(a, b, *, mode, tm, tn, tk, out_dtype, name, b_blocked=False,
            out_blocked=None, cols_outer=False, rider=None, after=()):
    a_shape = a.shape
    if mode == "nn":
        m_dim, k_dim = a_shape
        n_dim = b.shape[0] * b.shape[2] if b_blocked else b.shape[1]
        rows, cols, red = m_dim, n_dim, k_dim
    elif mode == "nt":
        m_dim, n_dim = a_shape
        k_dim = b.shape[1] if b_blocked else b.shape[0]
        rows, cols, red = m_dim, k_dim, n_dim
    else:
        m_dim, k_dim = a_shape
        n_dim = b.shape[1]
        rows, cols, red = k_dim, n_dim, m_dim
    assert rows % tm == 0 and cols % tn == 0 and red % tk == 0, (name, rows, cols, red)
    ni, nj, nk = rows // tm, cols // tn, red // tk

    if mode == "nn":
        a_spec = pl.BlockSpec((tm, tk), lambda i, j, k: (i, k))
        if b_blocked:
            per = b.shape[2] // tn
            b_spec = pl.BlockSpec((None, tk, tn), lambda i, j, k: (j // per, k, j % per))
        else:
            b_spec = pl.BlockSpec((tk, tn), lambda i, j, k: (k, j))
        dims = (((1,), (0,)), ((), ()))
    elif mode == "nt":
        a_spec = pl.BlockSpec((tm, tk), lambda i, j, k: (i, k))
        if b_blocked:
            per = b.shape[2] // tk
            b_spec = pl.BlockSpec((None, tn, tk), lambda i, j, k: (k // per, j, k % per))
        else:
            b_spec = pl.BlockSpec((tn, tk), lambda i, j, k: (j, k))
        dims = (((1,), (1,)), ((), ()))
    else:
        a_spec = pl.BlockSpec((tk, tm), lambda i, j, k: (k, i))
        b_spec = pl.BlockSpec((tk, tn), lambda i, j, k: (k, j))
        dims = (((0,), (0,)), ((), ()))

    if out_blocked:
        per_o = (cols // out_blocked) // tn
        out_spec = pl.BlockSpec((None, tm, tn), lambda i, j, k: (j // per_o, i, j % per_o))
        out_shape = jax.ShapeDtypeStruct((out_blocked, rows, cols // out_blocked), out_dtype)
    else:
        out_spec = pl.BlockSpec((tm, tn), lambda i, j, k: (i, j))
        out_shape = jax.ShapeDtypeStruct((rows, cols), out_dtype)

    def body(a_ref, b_ref, o_ref, *acc):
        prod = lax.dot_general(a_ref[...], b_ref[...], dims, preferred_element_type=F32)
        if nk == 1:
            o_ref[...] = prod.astype(out_dtype)
        else:
            acc_ref, = acc
            k = pl.program_id(2)

            @pl.when(k == 0)
            def _():
                acc_ref[...] = prod

            @pl.when(k > 0)
            def _():
                acc_ref[...] += prod

            @pl.when(k == nk - 1)
            def _():
                o_ref[...] = acc_ref[...].astype(out_dtype)

    scratch = [pltpu.VMEM((tm, tn), F32)] if nk > 1 else []
    grid = (ni, nj, nk)
    if cols_outer:
        swap = lambda spec: pl.BlockSpec(spec.block_shape, lambda j, i, k, f=spec.index_map: f(i, j, k))
        a_spec, b_spec, out_spec, grid = swap(a_spec), swap(b_spec), swap(out_spec), (nj, ni, nk)
    return _pallas(body, name=name, grid=grid, in_specs=[a_spec, b_spec], out_specs=out_spec,
                   out_shape=out_shape, operands=[a, b], scratch_shapes=scratch, rider=rider, after=after)


def _proj_by_slot(h, w_in, order, first, count, name, proj=None, after=()):
    s, k = h.shape
    nb = w_in.shape[2]
    tm = 512
    prev = [] if proj is None else [proj]

    def body(order_ref, h_ref, w_ref, *refs):
        refs[-1][...] = jnp.dot(h_ref[...], w_ref[...], preferred_element_type=F32)

    return pl.pallas_call(
        body, name=name,
        grid_spec=pltpu.PrefetchScalarGridSpec(
            num_scalar_prefetch=1, grid=(count, s // tm),
            in_specs=[pl.BlockSpec((tm, k), lambda j, i, order_ref: (i, 0)),
                      pl.BlockSpec((None, k, nb), lambda j, i, order_ref: (order_ref[first + j], 0, 0))]
            + [ANY] * (len(prev) + len(after)),
            out_specs=pl.BlockSpec((tm, nb), lambda j, i, order_ref: (i, order_ref[first + j]))),
        out_shape=jax.ShapeDtypeStruct((s, N_CHIPS * nb), F32),
        input_output_aliases={3: 0} if prev else {}, compiler_params=_params(),
    )(order, h, w_in, *prev, *after)


def _rmsnorm_fwd(x, w, name):
    s, d = x.shape
    tm = 256

    def body(x_ref, w_ref, o_ref):
        xv = x_ref[...]
        rstd = lax.rsqrt(jnp.mean(xv * xv, axis=-1, keepdims=True) + EPS)
        o_ref[...] = (xv * rstd * w_ref[...]).astype(BF16)

    return pl.pallas_call(
        body, name=name, grid=(s // tm,),
        in_specs=[pl.BlockSpec((tm, d), lambda i: (i, 0)), pl.BlockSpec((1, d), lambda i: (0, 0))],
        out_specs=pl.BlockSpec((tm, d), lambda i: (i, 0)),
        out_shape=jax.ShapeDtypeStruct((s, d), BF16), compiler_params=_params(),
    )(x, w)


def _matmul_nt_rmsnorm_bwd(dz, w_blocked, x, w, dres, name, rider=None, after=()):
    s, d = x.shape
    nk, _, nb = w_blocked.shape
    tm = 512
    nt_dims = (((1,), (1,)), ((), ()))

    def body(a_ref, b_ref, x_ref, w_ref, dres_ref, dx_ref, dxb_ref, dw_ref, acc_ref):
        k, i = pl.program_id(0), pl.program_id(1)
        rows = pl.ds(pl.multiple_of(i * tm, tm), tm)
        prod = lax.dot_general(a_ref[...], b_ref[...], nt_dims, preferred_element_type=F32)

        @pl.when(k == 0)
        def _():
            acc_ref[rows, :] = prod

        @pl.when(jnp.logical_and(k > 0, k < nk - 1))
        def _():
            acc_ref[rows, :] += prod

        @pl.when(k == nk - 1)
        def _():
            xv = x_ref[...]
            rstd = lax.rsqrt(jnp.mean(xv * xv, axis=-1, keepdims=True) + EPS)
            xhat = xv * rstd
            dhv = acc_ref[rows, :] + prod
            g = dhv * w_ref[...]
            dx = rstd * (g - xhat * jnp.mean(g * xhat, axis=-1, keepdims=True)) + dres_ref[...]
            dx_ref[...] = dx
            dxb_ref[...] = dx.astype(BF16)
            part = jnp.sum(dhv * xhat, axis=0, keepdims=True)

            @pl.when(i == 0)
            def _():
                dw_ref[...] = part

            @pl.when(i > 0)
            def _():
                dw_ref[...] += part

    assert nk >= 2
    row = pl.BlockSpec((tm, d), lambda k, i: (jnp.where(k == nk - 1, i, 0), 0))
    vec = pl.BlockSpec((1, d), lambda k, i: (0, 0))
    return _pallas(
        body, name=name, grid=(nk, s // tm),
        in_specs=[pl.BlockSpec((tm, nb), lambda k, i: (i, k)), pl.BlockSpec((None, d, nb), lambda k, i: (k, 0, 0)),
                  row, vec, row],
        out_specs=[row, row, vec],
        out_shape=[jax.ShapeDtypeStruct((s, d), F32), jax.ShapeDtypeStruct((s, d), BF16),
                   jax.ShapeDtypeStruct((1, d), F32)],
        operands=[dz, w_blocked, x, w, dres], scratch_shapes=[pltpu.VMEM((s, d), F32)], rider=rider, after=after)


def _rope_consts():
    lane = np.arange(LANES)
    in_head = lane % HEAD_DIM
    inv_freq = ROPE_THETA ** (-jnp.arange(0, ROT_DIM, 2, dtype=F32) / ROT_DIM)
    invf = jnp.where(jnp.asarray(in_head < ROT_DIM), jnp.tile(inv_freq, LANES // (ROT_DIM // 2)), 0.0)
    m_a = np.where(in_head < ROT_DIM // 2, -1.0, 0.0).astype(np.float32)
    m_b = np.where((in_head >= ROT_DIM // 2) & (in_head < ROT_DIM), 1.0, 0.0).astype(np.float32)
    block_diag = (lane[:, None] // HEAD_DIM == lane[None, :] // HEAD_DIM).astype(np.float32)
    return (invf.reshape(1, LANES).astype(F32), jnp.asarray(m_a).reshape(1, LANES),
            jnp.asarray(m_b).reshape(1, LANES), jnp.asarray(block_diag, dtype=BF16))


def _head_sums(v, bd):
    hi = v.astype(BF16)
    lo = (v - hi.astype(F32)).astype(BF16)
    return jnp.dot(hi, bd, preferred_element_type=F32) + jnp.dot(lo, bd, preferred_element_type=F32)


def _rope_tables(pos_col, consts, name, after=()):
    s = pos_col.shape[0]
    tm = 512
    invf, m_a, m_b, _ = consts

    def body(pos_ref, invf_ref, ma_ref, mb_ref, cos_ref, sa_ref, sb_ref):
        ang = pos_ref[...].astype(F32) * invf_ref[...]
        sin = jnp.sin(ang)
        cos_ref[...] = jnp.cos(ang)
        sa_ref[...] = sin * ma_ref[...]
        sb_ref[...] = sin * mb_ref[...]

    vec = pl.BlockSpec((1, LANES), lambda i: (0, 0))
    tab = pl.BlockSpec((tm, LANES), lambda i: (i, 0))
    return _pallas(body, name=name, grid=(s // tm,), in_specs=[pl.BlockSpec((tm, 1), lambda i: (i, 0)), vec, vec, vec],
                   out_specs=[tab] * 3, out_shape=[jax.ShapeDtypeStruct((s, LANES), F32)] * 3,
                   operands=[pos_col, invf, m_a, m_b], after=after)


def _qk_fwd(proj, rope, qw2, kw2, bd, name, rider=None):
    s = proj.shape[0]
    width = 3 * N_SLOT_HEADS * HEAD_DIM
    tm = 128
    scale = HEAD_DIM ** -0.5

    def body(q_ref, k_ref, cos_ref, sa_ref, sb_ref, qw_ref, kw_ref, bd_ref, qo_ref, ko_ref):
        cos, s_a, s_b = cos_ref[...], sa_ref[...], sb_ref[...]
        bdv = bd_ref[...]
        for src, w_ref, dst, sc in ((q_ref, qw_ref, qo_ref, scale), (k_ref, kw_ref, ko_ref, 1.0)):
            for cb in range(width // LANES):
                cols = slice(cb * LANES, (cb + 1) * LANES)
                t = src[:, cols]
                rstd = lax.rsqrt(_head_sums(t * t, bdv) * (1.0 / HEAD_DIM) + EPS)
                y = t * rstd * w_ref[...]
                r = y * cos + pltpu.roll(y, LANES - 8, axis=1) * s_a + pltpu.roll(y, 8, axis=1) * s_b
                dst[:, cols] = r * sc if sc != 1.0 else r

    vec = pl.BlockSpec((1, LANES), lambda i: (0, 0))
    tab = pl.BlockSpec((tm, LANES), lambda i: (i, 0))
    return _pallas(
        body, name=name, grid=(s // tm,),
        in_specs=[pl.BlockSpec((tm, width), lambda i: (i, 0)), pl.BlockSpec((tm, width), lambda i: (i, 1)),
                  tab, tab, tab, vec, vec, pl.BlockSpec((LANES, LANES), lambda i: (0, 0))],
        out_specs=[pl.BlockSpec((tm, width), lambda i: (i, 0))] * 2,
        out_shape=[jax.ShapeDtypeStruct((s, width), F32)] * 2,
        operands=[proj, proj, *rope, qw2, kw2, bd], rider=rider)


def _qk_bwd(dqn, dkn, dv, da, db, dgl, proj, rope, qw2, kw2, bd, name, rider=None):
    s = proj.shape[0]
    width = 3 * N_SLOT_HEADS * HEAD_DIM
    ch = da.shape[1]
    gate_w = dgl.shape[1]
    out_w = 3 * width + 2 * ch + gate_w
    assert out_w == proj.shape[1]
    tm = 128
    scale = HEAD_DIM ** -0.5

    def body(dq_ref, dk_ref, dv_ref, da_ref, db_ref, dgl_ref, q_ref, k_ref, cos_ref, sa_ref, sb_ref, qw_ref, kw_ref,
             bd_ref, out_ref, dqw_ref, dkw_ref):
        cos, s_a, s_b = cos_ref[...], sa_ref[...], sb_ref[...]
        bdv = bd_ref[...]
        first = pl.program_id(0) == 0
        for src, dsrc, w_ref, col0, dw_ref, sc in ((q_ref, dq_ref, qw_ref, 0, dqw_ref, scale),
                                                   (k_ref, dk_ref, kw_ref, width, dkw_ref, 1.0)):
            dw_acc = jnp.zeros((1, LANES), F32)
            for cb in range(width // LANES):
                cols = slice(cb * LANES, (cb + 1) * LANES)
                t = src[:, cols]
                dr = dsrc[:, cols]
                if sc != 1.0:
                    dr = dr * sc
                dy = dr * cos + pltpu.roll(dr * s_a, 8, axis=1) + pltpu.roll(dr * s_b, LANES - 8, axis=1)
                rstd = lax.rsqrt(_head_sums(t * t, bdv) * (1.0 / HEAD_DIM) + EPS)
                xhat = t * rstd
                g = dy * w_ref[...]
                dt = rstd * (g - xhat * (_head_sums(g * xhat, bdv) * (1.0 / HEAD_DIM)))
                out_ref[:, col0 + cb * LANES: col0 + (cb + 1) * LANES] = dt.astype(BF16)
                dw_acc = dw_acc + jnp.sum(dy * xhat, axis=0, keepdims=True)
            dw_acc = dw_acc + pltpu.roll(dw_acc, HEAD_DIM, axis=1)

            @pl.when(first)
            def _(dw_ref=dw_ref, dw_acc=dw_acc):
                dw_ref[...] = dw_acc

            @pl.when(jnp.logical_not(first))
            def _(dw_ref=dw_ref, dw_acc=dw_acc):
                dw_ref[...] += dw_acc
        out_ref[:, 2 * width: 3 * width] = dv_ref[...].astype(BF16)
        out_ref[:, 3 * width: 3 * width + ch] = da_ref[...]
        out_ref[:, 3 * width + ch: 3 * width + 2 * ch] = db_ref[...]
        out_ref[:, 3 * width + 2 * ch: out_w] = dgl_ref[...]

    vec = pl.BlockSpec((1, LANES), lambda i: (0, 0))
    blk = lambda c: pl.BlockSpec((tm, width), lambda i: (i, c))
    cblk = pl.BlockSpec((tm, ch), lambda i: (i, 0))
    tab = pl.BlockSpec((tm, LANES), lambda i: (i, 0))
    return _pallas(
        body, name=name, grid=(s // tm,),
        in_specs=[blk(0), blk(0), blk(0), cblk, cblk, pl.BlockSpec((tm, gate_w), lambda i: (i, 0)),
                  blk(0), blk(1), tab, tab, tab, vec, vec, pl.BlockSpec((LANES, LANES), lambda i: (0, 0))],
        out_specs=[pl.BlockSpec((tm, out_w), lambda i: (i, 0)), vec, vec],
        out_shape=[jax.ShapeDtypeStruct((s, out_w), BF16)] + [jax.ShapeDtypeStruct((1, LANES), F32)] * 2,
        operands=[dqn, dkn, dv, da, db, dgl, proj, proj, *rope, qw2, kw2, bd], rider=rider)


def _row_chunks(n_rows, fn, chunk=256):
    def step(i, c):
        fn(pl.ds(pl.multiple_of(i * chunk, chunk), chunk))
        return c
    lax.fori_loop(0, n_rows // chunk, step, 0)


def _to_residue_major(dst, src, s, d, dst_off=0, cast=None):
    seq = s // d
    for r in range(d):
        v = src[...] if d == 1 else src[pl.ds(r, seq, stride=d), :]
        dst[dst_off + r * seq: dst_off + (r + 1) * seq, :] = v if cast is None else v.astype(cast)


def _from_residue_major(dst, src, s, d, src_off=0):
    seq = s // d
    for r in range(d):
        v = src[src_off + r * seq: src_off + (r + 1) * seq, :]
        if d == 1:
            dst[...] = v
        else:
            dst[pl.ds(r, seq, stride=d), :] = v


def _band_bias():
    qi = lax.broadcasted_iota(jnp.int32, (QBLK, KWIN), 0)
    kj = lax.broadcasted_iota(jnp.int32, (QBLK, KWIN), 1)
    return jnp.where(jnp.abs(kj - HALF_SPAN - qi) <= HALF_SPAN, 0.0, NEG_INF).astype(F32)


def _range_bias(base, seq):
    kj = lax.broadcasted_iota(jnp.int32, (1, KWIN), 1)
    lo = (base & -seq) - base + HALF_SPAN
    return jnp.where((kj >= lo) & (kj < lo + seq), 0.0, NEG_INF).astype(F32)


def _skewed_blocks(n_blk, produce, consume):
    produce(0, 0)
    for b in range(n_blk):
        consume(b, b % 2)
        if b + 1 < n_blk:
            produce(b + 1, (b + 1) % 2)


def _block_base(b):
    return b * QBLK if isinstance(b, int) else pl.multiple_of(b * QBLK, QBLK)


def _attn_fwd(qn, kn, proj, name, rider=None):
    s = qn.shape[0]
    n_pairs = N_SLOT_HEADS * HEAD_DIM // LANES
    v_col0 = 2 * qn.shape[1] // LANES
    nt_dims = (((1,), (1,)), ((), ()))

    def body(q_ref, k_ref, v_ref, attn_ref, lse_ref, attn_b_ref, q_rm, k_rm, v_rm, acc_rm, m_rm, l_rm,
             acc_p, m_p, l_p, m_run, l_run, acc_run, band, s_buf, m_buf):
        g = pl.program_id(1)
        zpad = jnp.zeros((HALF_SPAN, LANES), BF16)
        k_rm[0:HALF_SPAN, :] = zpad
        k_rm[s + HALF_SPAN: s + 2 * HALF_SPAN, :] = zpad
        v_rm[0:HALF_SPAN, 0:LANES] = zpad
        v_rm[s + HALF_SPAN: s + 2 * HALF_SPAN, 0:LANES] = zpad

        def ones_rows(rows):
            v_rm[pl.ds(rows.start, rows.size), LANES:2 * LANES] = jnp.ones((rows.size, LANES), BF16)

        _row_chunks(s + 2 * HALF_SPAN, ones_rows, chunk=2 * HALF_SPAN)
        band[...] = _band_bias()
        lane = lax.broadcasted_iota(jnp.int32, (QBLK, LANES), 1)
        low = lane < HEAD_DIM
        n_blk = s // QBLK

        for gi, d in enumerate(DILATIONS):
            @pl.when(g == gi)
            def _(gi=gi, d=d):
                seq = s // d
                _to_residue_major(q_rm, q_ref, s, d, cast=BF16)
                _to_residue_major(k_rm, k_ref, s, d, dst_off=HALF_SPAN, cast=BF16)
                _to_residue_major(v_rm.at[:, 0:LANES], v_ref, s, d, dst_off=HALF_SPAN, cast=BF16)

                def scores(b, slot):
                    base = _block_base(b)
                    q = q_rm[pl.ds(base, QBLK), :]
                    zero = jnp.zeros_like(q)
                    q2 = jnp.concatenate([jnp.where(low, q, zero), jnp.where(low, zero, q)], axis=0)
                    sc = lax.dot_general(q2, k_rm[pl.ds(base, KWIN), :], nt_dims, preferred_element_type=F32)
                    bias = band[...] + _range_bias(base, seq)
                    for hh in range(2):
                        rows = slice(hh * QBLK, (hh + 1) * QBLK)
                        sh = sc[rows, :] + bias
                        s_buf[slot, rows, :] = sh
                        m_buf[slot, rows, :] = jnp.broadcast_to(jnp.max(sh, axis=-1, keepdims=True), (QBLK, LANES))

                def outputs(b, slot):
                    base = _block_base(b)
                    sv = s_buf[slot]
                    mb = m_buf[slot]
                    p = jnp.exp(jnp.concatenate([sv[:, 0:LANES] - mb, sv[:, LANES:2 * LANES] - mb], axis=1))
                    pv = jnp.dot(p.astype(BF16), v_rm[pl.ds(base, KWIN), :], preferred_element_type=F32)
                    rows = pl.ds(base, QBLK)
                    acc_rm[rows, :] = jnp.where(low, pv[0:QBLK, 0:LANES], pv[QBLK:2 * QBLK, 0:LANES])
                    l_rm[rows, :] = jnp.where(low, pv[0:QBLK, LANES:2 * LANES], pv[QBLK:2 * QBLK, LANES:2 * LANES])
                    m_rm[rows, :] = jnp.where(low, mb[0:QBLK, :], mb[QBLK:2 * QBLK, :])

                _skewed_blocks(n_blk, scores, outputs)
                if d == 1:
                    src = (acc_rm, m_rm, l_rm)
                else:
                    for dst_, src_ in ((acc_p, acc_rm), (m_p, m_rm), (l_p, l_rm)):
                        _from_residue_major(dst_, src_, s, d)
                    src = (acc_p, m_p, l_p)

                def combine(rows):
                    a_g, m_g, l_g = src[0][rows, :], src[1][rows, :], src[2][rows, :]
                    if gi == 0:
                        m_new, l_new, a_new = m_g, l_g, a_g
                    else:
                        m_old = m_run[rows, :]
                        m_new = jnp.maximum(m_old, m_g)
                        w_old = jnp.exp(m_old - m_new)
                        w_g = jnp.exp(m_g - m_new)
                        l_new = l_run[rows, :] * w_old + l_g * w_g
                        a_new = acc_run[rows, :] * w_old + a_g * w_g
                    if gi == len(DILATIONS) - 1:
                        out = a_new / l_new
                        attn_ref[rows, :] = out
                        attn_b_ref[rows, :] = out.astype(BF16)
                        lse_ref[rows, :] = m_new + jnp.log(l_new)
                    else:
                        m_run[rows, :] = m_new
                        l_run[rows, :] = l_new
                        acc_run[rows, :] = a_new

                _row_chunks(s, combine)

    qk_spec = pl.BlockSpec((s, LANES), lambda hp, g: (0, g * n_pairs + hp))
    v_spec = pl.BlockSpec((s, LANES), lambda hp, g: (0, v_col0 + g * n_pairs + hp))
    o_spec = pl.BlockSpec((s, LANES), lambda hp, g: (0, hp))
    f32buf = pltpu.VMEM((s, LANES), F32)
    return _pallas(
        body, name=name, grid=(n_pairs, len(DILATIONS)), in_specs=[qk_spec, qk_spec, v_spec],
        out_specs=[o_spec, o_spec, o_spec],
        out_shape=[jax.ShapeDtypeStruct((s, n_pairs * LANES), F32)] * 2
        + [jax.ShapeDtypeStruct((s, n_pairs * LANES), BF16)],
        operands=[qn, kn, proj],
        scratch_shapes=[pltpu.VMEM((s, LANES), BF16), pltpu.VMEM((s + 2 * HALF_SPAN, LANES), BF16),
                        pltpu.VMEM((s + 2 * HALF_SPAN, 2 * LANES), BF16)] + [f32buf] * 9
        + [pltpu.VMEM((QBLK, KWIN), F32), pltpu.VMEM((2, 2 * QBLK, KWIN), F32),
           pltpu.VMEM((2, 2 * QBLK, LANES), F32)],
        rider=rider)


def _attn_bwd(qn, kn, proj, dattn, attn, lse, bd, name, rider=None):
    s = qn.shape[0]
    n_pairs = N_SLOT_HEADS * HEAD_DIM // LANES
    v_col0 = 2 * qn.shape[1] // LANES
    nt_dims = (((1,), (1,)), ((), ()))
    tn_dims = (((0,), (0,)), ((), ()))
    spad = s + 2 * HALF_SPAN

    def body(q_ref, k_ref, v_ref, do_ref, o_ref, lse_ref, bd_ref, dq_ref, dk_ref, dv_ref,
             q_rm, k_rm, v_rm, do_rm, lse0_rm, lse1_rm, dd0_rm, dd1_rm, dq_rm, dk_rm, dv_rm,
             lse0_p, lse1_p, dd0_p, dd1_p, band, p_buf, ds_buf):
        g = pl.program_id(1)
        zpad = jnp.zeros((HALF_SPAN, LANES), BF16)
        for buf in (k_rm, v_rm):
            buf[0:HALF_SPAN, :] = zpad
            buf[s + HALF_SPAN: spad, :] = zpad
        zf = jnp.zeros((HALF_SPAN, LANES), F32)
        for buf in (dk_rm, dv_rm):
            buf[0:HALF_SPAN, :] = zf
            buf[s + HALF_SPAN: spad, :] = zf
        band[...] = _band_bias()

        def clear(rows):
            z = jnp.zeros((rows.size, LANES), F32)
            dk_rm[pl.ds(rows.start + HALF_SPAN, rows.size), :] = z
            dv_rm[pl.ds(rows.start + HALF_SPAN, rows.size), :] = z

        _row_chunks(s, clear)

        def prepare(rows):
            lo = lax.broadcasted_iota(jnp.int32, (rows.size, LANES), 1) < HEAD_DIM
            dsum = _head_sums(do_ref[rows, :] * o_ref[rows, :], bd_ref[...])
            dswap = pltpu.roll(dsum, HEAD_DIM, axis=1)
            dd0_p[rows, :] = jnp.where(lo, dsum, dswap)
            dd1_p[rows, :] = jnp.where(lo, dswap, dsum)
            lv = lse_ref[rows, :]
            lswap = pltpu.roll(lv, HEAD_DIM, axis=1)
            lse0_p[rows, :] = jnp.where(lo, lv, lswap)
            lse1_p[rows, :] = jnp.where(lo, lswap, lv)

        @pl.when(g == 0)
        def _():
            _row_chunks(s, prepare)
        lane = lax.broadcasted_iota(jnp.int32, (QBLK, LANES), 1)
        low = lane < HEAD_DIM
        n_blk = s // QBLK

        def stacked(ref, rows):
            val = ref[rows, :]
            zero = jnp.zeros_like(val)
            return jnp.concatenate([jnp.where(low, val, zero), jnp.where(low, zero, val)], axis=0)

        for gi, d in enumerate(DILATIONS):
            @pl.when(g == gi)
            def _(d=d):
                seq = s // d
                _to_residue_major(q_rm, q_ref, s, d, cast=BF16)
                _to_residue_major(k_rm, k_ref, s, d, dst_off=HALF_SPAN, cast=BF16)
                _to_residue_major(v_rm, v_ref, s, d, dst_off=HALF_SPAN, cast=BF16)
                _to_residue_major(do_rm, do_ref, s, d, cast=BF16)
                for dst_, src_ in ((lse0_rm, lse0_p), (lse1_rm, lse1_p), (dd0_rm, dd0_p), (dd1_rm, dd1_p)):
                    _to_residue_major(dst_, src_, s, d)

                def scores(b, slot):
                    base = _block_base(b)
                    rows = pl.ds(base, QBLK)
                    win = pl.ds(base, KWIN)
                    sc = lax.dot_general(stacked(q_rm, rows), k_rm[win, :], nt_dims, preferred_element_type=F32)
                    dp = lax.dot_general(stacked(do_rm, rows), v_rm[win, :], nt_dims, preferred_element_type=F32)
                    bias = band[...] + _range_bias(base, seq)
                    for hh, (lse_r, dd_r) in enumerate(((lse0_rm, dd0_rm), (lse1_rm, dd1_rm))):
                        r = slice(hh * QBLK, (hh + 1) * QBLK)
                        lse_h = lse_r[rows, :]
                        dd_h = dd_r[rows, :]
                        sh = sc[r, :] + bias
                        p = jnp.exp(jnp.concatenate([sh[:, 0:LANES] - lse_h, sh[:, LANES:KWIN] - lse_h], axis=1))
                        dph = dp[r, :]
                        ds = p * jnp.concatenate([dph[:, 0:LANES] - dd_h, dph[:, LANES:KWIN] - dd_h], axis=1)
                        p_buf[slot, r, :] = p.astype(BF16)
                        ds_buf[slot, r, :] = ds.astype(BF16)

                def grads(b, slot):
                    base = _block_base(b)
                    rows = pl.ds(base, QBLK)
                    win = pl.ds(base, KWIN)
                    p = p_buf[slot]
                    ds = ds_buf[slot]
                    dq2 = jnp.dot(ds, k_rm[win, :], preferred_element_type=F32)
                    dq_rm[rows, :] = jnp.where(low, dq2[0:QBLK, :], dq2[QBLK:2 * QBLK, :])
                    dk_rm[win, :] += lax.dot_general(ds, stacked(q_rm, rows), tn_dims, preferred_element_type=F32)
                    dv_rm[win, :] += lax.dot_general(p, stacked(do_rm, rows), tn_dims, preferred_element_type=F32)

                _skewed_blocks(n_blk, scores, grads)
                _from_residue_major(dq_ref, dq_rm, s, d)
                _from_residue_major(dk_ref, dk_rm, s, d, src_off=HALF_SPAN)
                _from_residue_major(dv_ref, dv_rm, s, d, src_off=HALF_SPAN)

    qk_spec = pl.BlockSpec((s, LANES), lambda hp, g: (0, g * n_pairs + hp))
    v_spec = pl.BlockSpec((s, LANES), lambda hp, g: (0, v_col0 + g * n_pairs + hp))
    o_spec = pl.BlockSpec((s, LANES), lambda hp, g: (0, hp))
    width = qn.shape[1]
    f32buf = pltpu.VMEM((s, LANES), F32)
    f32pad = pltpu.VMEM((spad, LANES), F32)
    return _pallas(
        body, name=name, grid=(n_pairs, len(DILATIONS)),
        in_specs=[qk_spec, qk_spec, v_spec, o_spec, o_spec, o_spec,
                  pl.BlockSpec((LANES, LANES), lambda hp, g: (0, 0))],
        out_specs=[qk_spec, qk_spec, qk_spec],
        out_shape=[jax.ShapeDtypeStruct((s, width), F32)] * 3,
        operands=[qn, kn, proj, dattn, attn, lse, bd],
        scratch_shapes=[pltpu.VMEM((s, LANES), BF16), pltpu.VMEM((spad, LANES), BF16),
                        pltpu.VMEM((spad, LANES), BF16), pltpu.VMEM((s, LANES), BF16),
                        f32buf, f32buf, f32buf, f32buf, f32buf, f32pad, f32pad,
                        f32buf, f32buf, f32buf, f32buf, pltpu.VMEM((QBLK, KWIN), F32),
                        pltpu.VMEM((2, 2 * QBLK, KWIN), BF16), pltpu.VMEM((2, 2 * QBLK, KWIN), BF16)],
        rider=rider)


CONV_PAD = 16


def _conv_fwd(proj, conv_w, conv_b, col0, name, rider=None):
    s = proj.shape[0]
    ch = conv_w.shape[1]
    nblk = ch // LANES
    a0 = col0 // LANES
    tr = 256
    shift = CONV_PAD - (CONV_WIDTH - 1) // 2

    def body(a_ref, b_ref, w_ref, bias_ref, u0_ref, uc_ref, pad):
        z = jnp.zeros((CONV_PAD, LANES), F32)
        pad[0:CONV_PAD, :] = z
        pad[s + CONV_PAD: s + 2 * CONV_PAD, :] = z

        def glu(rows):
            u0 = a_ref[rows, :] * jax.nn.sigmoid(b_ref[rows, :])
            u0_ref[rows, :] = u0
            pad[pl.ds(rows.start + CONV_PAD, rows.size), :] = u0

        _row_chunks(s, glu)
        for t in range(0, s, tr):
            acc = jnp.broadcast_to(bias_ref[...], (tr, LANES))
            for k in range(CONV_WIDTH):
                acc = acc + w_ref[k:k + 1, :] * pad[t + k + shift: t + k + shift + tr, :]
            uc_ref[t:t + tr, :] = acc

    return _pallas(
        body, name=name, grid=(nblk,),
        in_specs=[pl.BlockSpec((s, LANES), lambda c: (0, a0 + c)),
                  pl.BlockSpec((s, LANES), lambda c: (0, a0 + nblk + c)),
                  pl.BlockSpec((CONV_WIDTH, LANES), lambda c: (0, c)),
                  pl.BlockSpec((1, LANES), lambda c: (0, c))],
        out_specs=[pl.BlockSpec((s, LANES), lambda c: (0, c))] * 2,
        out_shape=[jax.ShapeDtypeStruct((s, ch), F32)] * 2, operands=[proj, proj, conv_w, conv_b],
        scratch_shapes=[pltpu.VMEM((s + 2 * CONV_PAD, LANES), F32)], rider=rider)


def _ln_silu_fwd(uc, ln_w, ln_b, name):
    s, ch = uc.shape
    tm = 256

    def body(u_ref, w_ref, b_ref, o_ref):
        u = u_ref[...]
        mu = jnp.mean(u, axis=-1, keepdims=True)
        xc = u - mu
        rstd = lax.rsqrt(jnp.mean(xc * xc, axis=-1, keepdims=True) + EPS)
        z = xc * rstd * w_ref[...] + b_ref[...]
        o_ref[...] = (z * jax.nn.sigmoid(z)).astype(BF16)

    row = pl.BlockSpec((tm, ch), lambda i: (i, 0))
    vec = pl.BlockSpec((1, ch), lambda i: (0, 0))
    return pl.pallas_call(
        body, name=name, grid=(s // tm,), in_specs=[row, vec, vec], out_specs=row,
        out_shape=jax.ShapeDtypeStruct((s, ch), BF16), compiler_params=_params(),
    )(uc, ln_w, ln_b)


def _ln_silu_bwd(du3, uc, ln_w, ln_b, name):
    s, ch = uc.shape
    tm = 256

    def body(d_ref, u_ref, w_ref, b_ref, du_ref, dw_ref, db_ref):
        u = u_ref[...]
        mu = jnp.mean(u, axis=-1, keepdims=True)
        xc = u - mu
        rstd = lax.rsqrt(jnp.mean(xc * xc, axis=-1, keepdims=True) + EPS)
        xhat = xc * rstd
        z = xhat * w_ref[...] + b_ref[...]
        sg = jax.nn.sigmoid(z)
        dz = d_ref[...] * (sg * (1.0 + z * (1.0 - sg)))
        dxh = dz * w_ref[...]
        du_ref[...] = rstd * (dxh - jnp.mean(dxh, axis=-1, keepdims=True)
                              - xhat * jnp.mean(dxh * xhat, axis=-1, keepdims=True))
        pw = jnp.sum(dz * xhat, axis=0, keepdims=True)
        pb = jnp.sum(dz, axis=0, keepdims=True)
        first = pl.program_id(0) == 0

        @pl.when(first)
        def _():
            dw_ref[...] = pw
            db_ref[...] = pb

        @pl.when(jnp.logical_not(first))
        def _():
            dw_ref[...] += pw
            db_ref[...] += pb

    row = pl.BlockSpec((tm, ch), lambda i: (i, 0))
    vec = pl.BlockSpec((1, ch), lambda i: (0, 0))
    return pl.pallas_call(
        body, name=name, grid=(s // tm,), in_specs=[row, row, vec, vec], out_specs=[row, vec, vec],
        out_shape=[jax.ShapeDtypeStruct((s, ch), F32), jax.ShapeDtypeStruct((1, ch), F32),
                   jax.ShapeDtypeStruct((1, ch), F32)],
        compiler_params=_params(),
    )(du3, uc, ln_w, ln_b)


def _conv_bwd(duc, u0, proj, conv_w, col0, name, rider=None):
    s = proj.shape[0]
    ch = conv_w.shape[1]
    nblk = ch // LANES
    a0 = col0 // LANES
    tr = 256
    half = (CONV_WIDTH - 1) // 2
    shift = CONV_PAD - half

    def body(duc_ref, u0_ref, a_ref, b_ref, w_ref, da_ref, db_ref, dw_ref, dbias_ref, pad_d, pad_u):
        z = jnp.zeros((CONV_PAD, LANES), F32)
        for buf in (pad_d, pad_u):
            buf[0:CONV_PAD, :] = z
            buf[s + CONV_PAD: s + 2 * CONV_PAD, :] = z

        def fill(rows):
            dst = pl.ds(rows.start + CONV_PAD, rows.size)
            pad_d[dst, :] = duc_ref[rows, :]
            pad_u[dst, :] = u0_ref[rows, :]

        _row_chunks(s, fill)
        dw_acc = [jnp.zeros((8, LANES), F32) for _ in range(CONV_WIDTH)]
        dbias_acc = jnp.zeros((8, LANES), F32)
        for t in range(0, s, tr):
            d_t = duc_ref[t:t + tr, :]
            dbias_acc = dbias_acc + jnp.sum(d_t.reshape(tr // 8, 8, LANES), axis=0)
            du0 = jnp.zeros((tr, LANES), F32)
            for k in range(CONV_WIDTH):
                du0 = du0 + w_ref[k:k + 1, :] * pad_d[t - k + half + CONV_PAD: t - k + half + CONV_PAD + tr, :]
                prod = d_t * pad_u[t + k + shift: t + k + shift + tr, :]
                dw_acc[k] = dw_acc[k] + jnp.sum(prod.reshape(tr // 8, 8, LANES), axis=0)
            av = a_ref[t:t + tr, :]
            sg = jax.nn.sigmoid(b_ref[t:t + tr, :])
            da_ref[t:t + tr, :] = (du0 * sg).astype(BF16)
            db_ref[t:t + tr, :] = (du0 * av * sg * (1.0 - sg)).astype(BF16)
        for k in range(CONV_WIDTH):
            dw_ref[k:k + 1, :] = jnp.sum(dw_acc[k], axis=0, keepdims=True)
        dbias_ref[...] = jnp.sum(dbias_acc, axis=0, keepdims=True)

    col = lambda off: pl.BlockSpec((s, LANES), lambda c: (0, off + c))
    return _pallas(
        body, name=name, grid=(nblk,),
        in_specs=[col(0), col(0), col(a0), col(a0 + nblk),
                  pl.BlockSpec((CONV_WIDTH, LANES), lambda c: (0, c))],
        out_specs=[col(0), col(0), pl.BlockSpec((CONV_WIDTH, LANES), lambda c: (0, c)),
                   pl.BlockSpec((1, LANES), lambda c: (0, c))],
        out_shape=[jax.ShapeDtypeStruct((s, ch), BF16)] * 2
        + [jax.ShapeDtypeStruct((CONV_WIDTH, ch), F32), jax.ShapeDtypeStruct((1, ch), F32)],
        operands=[duc, u0, proj, proj, conv_w],
        scratch_shapes=[pltpu.VMEM((s + 2 * CONV_PAD, LANES), F32)] * 2, rider=rider)


def _mix_out_proj(attn_b, u3, w_o, w_pw, proj, bg, col0, w_out, x, norm_w, name, rider=None):
    s, d = x.shape
    k = attn_b.shape[1]
    tm = 256
    half = d // 2
    assert col0 % half == 0
    c0 = col0 // half

    def body(a_ref, u_ref, wo_ref, wp_ref, a0_ref, a1_ref, b0_ref, b1_ref, bias_ref, w_ref, x_ref, nw_ref,
             ya_ref, yb_ref, mixed_ref, x1_ref, h2_ref):
        mixed = None
        for br, (src_ref, wb_ref, lo_ref, hi_ref, y_ref) in enumerate(((a_ref, wo_ref, a0_ref, a1_ref, ya_ref),
                                                                       (u_ref, wp_ref, b0_ref, b1_ref, yb_ref))):
            src = src_ref[...]
            y = jnp.concatenate([jnp.dot(src, wb_ref[j], preferred_element_type=F32) for j in range(N_CHIPS)],
                                axis=1)
            y_ref[...] = y
            logits = jnp.concatenate([lo_ref[...], hi_ref[...]], axis=1)
            part = jax.nn.sigmoid(logits + bias_ref[br]) * y
            mixed = part if mixed is None else mixed + part
        mixed = mixed.astype(BF16)
        mixed_ref[...] = mixed
        x1 = x_ref[...] + jnp.dot(mixed, w_ref[...], preferred_element_type=F32)
        x1_ref[...] = x1
        rstd = lax.rsqrt(jnp.mean(x1 * x1, axis=-1, keepdims=True) + EPS)
        h2_ref[...] = (x1 * rstd * nw_ref[...]).astype(BF16)

    row = pl.BlockSpec((tm, d), lambda i: (i, 0))
    src_row = pl.BlockSpec((tm, k), lambda i: (i, 0))
    blocks = pl.BlockSpec(w_o.shape, lambda i: (0, 0, 0))
    logit_blk = lambda j: pl.BlockSpec((tm, half), functools.partial(lambda i, j: (i, c0 + j), j=j))
    return _pallas(
        body, name=name, grid=(s // tm,),
        in_specs=[src_row, src_row, blocks, blocks, logit_blk(0), logit_blk(1), logit_blk(2), logit_blk(3),
                  pl.BlockSpec((2, 1, d), lambda i: (0, 0, 0)), pl.BlockSpec((d, d), lambda i: (0, 0)), row,
                  pl.BlockSpec((1, d), lambda i: (0, 0))],
        out_specs=[row] * 5,
        out_shape=[jax.ShapeDtypeStruct((s, d), F32), jax.ShapeDtypeStruct((s, d), F32),
                   jax.ShapeDtypeStruct((s, d), BF16), jax.ShapeDtypeStruct((s, d), F32),
                   jax.ShapeDtypeStruct((s, d), BF16)],
        operands=[attn_b, u3, w_o, w_pw, proj, proj, proj, proj, bg, w_out, x, norm_w], rider=rider)


def _out_proj_bwd_gates(dx1, w_out, proj, bg, y_a, y_b, w_o, w_pw, col0, name, after=()):
    s, d = y_a.shape
    n_blk, k, blk = w_o.shape
    tm = 256
    half = d // 2
    assert col0 % half == 0
    c0 = col0 // half
    nt_dims = (((1,), (1,)), ((), ()))

    def body(dx_ref, w_ref, a0_ref, a1_ref, b0_ref, b1_ref, bias_ref, ya_ref, yb_ref, wo_ref, wp_ref,
             dgl_ref, dya_ref, dyb_ref, db_ref, da_ref, du_ref):
        dm = lax.dot_general(dx_ref[...], w_ref[...], nt_dims, preferred_element_type=F32)
        parts = []
        for br, (lo_ref, hi_ref, y_ref, dy_ref, wb_ref, dsrc_ref) in enumerate((
                (a0_ref, a1_ref, ya_ref, dya_ref, wo_ref, da_ref), (b0_ref, b1_ref, yb_ref, dyb_ref, wp_ref, du_ref))):
            logits = jnp.concatenate([lo_ref[...], hi_ref[...]], axis=1)
            gate = jax.nn.sigmoid(logits + bias_ref[br])
            dy = (dm * gate).astype(BF16)
            dy_ref[...] = dy
            dsrc = lax.dot_general(dy[:, 0:blk], wb_ref[0], nt_dims, preferred_element_type=F32)
            for j in range(1, n_blk):
                dsrc = dsrc + lax.dot_general(dy[:, j * blk:(j + 1) * blk], wb_ref[j], nt_dims,
                                              preferred_element_type=F32)
            dsrc_ref[...] = dsrc
            dgl = dm * y_ref[...] * gate * (1.0 - gate)
            dgl_ref[:, br * d:(br + 1) * d] = dgl.astype(BF16)
            parts.append(jnp.sum(dgl, axis=0, keepdims=True))
        part = jnp.concatenate(parts, axis=0)
        first = pl.program_id(0) == 0

        @pl.when(first)
        def _():
            db_ref[...] = part

        @pl.when(jnp.logical_not(first))
        def _():
            db_ref[...] += part

    row = pl.BlockSpec((tm, d), lambda i: (i, 0))
    logit_blk = lambda k: pl.BlockSpec((tm, half), functools.partial(lambda i, k: (i, c0 + k), k=k))
    blocks = pl.BlockSpec(w_o.shape, lambda i: (0, 0, 0))
    src_row = pl.BlockSpec((tm, k), lambda i: (i, 0))
    return _pallas(
        body, name=name, grid=(s // tm,),
        in_specs=[row, pl.BlockSpec((d, d), lambda i: (0, 0)), logit_blk(0), logit_blk(1), logit_blk(2),
                  logit_blk(3), pl.BlockSpec((2, 1, d), lambda i: (0, 0, 0)), row, row, blocks, blocks],
        out_specs=[pl.BlockSpec((tm, 2 * d), lambda i: (i, 0)), row, row, pl.BlockSpec((2, d), lambda i: (0, 0)),
                   src_row, src_row],
        out_shape=[jax.ShapeDtypeStruct((s, 2 * d), BF16), jax.ShapeDtypeStruct((s, d), BF16),
                   jax.ShapeDtypeStruct((s, d), BF16), jax.ShapeDtypeStruct((2, d), F32),
                   jax.ShapeDtypeStruct((s, k), F32), jax.ShapeDtypeStruct((s, k), F32)],
        operands=[dx1, w_out, proj, proj, proj, proj, bg, y_a, y_b, w_o, w_pw], after=after)


def _ffn_in_swiglu(h2, w_blocked, name):
    s, k = h2.shape
    nblk, _, tn = w_blocked.shape
    ff = nblk // 2 * tn
    tm = 512

    def body(a_ref, wg_ref, wu_ref, g_ref, u_ref, act_ref):
        a = a_ref[...]
        gt = jnp.dot(a, wg_ref[...], preferred_element_type=F32)
        up = jnp.dot(a, wu_ref[...], preferred_element_type=F32)
        g_ref[...] = gt
        u_ref[...] = up
        act_ref[...] = (gt * jax.nn.sigmoid(gt) * up).astype(BF16)

    out = pl.BlockSpec((tm, tn), lambda j, i: (i, j))
    return pl.pallas_call(
        body, name=name, grid=(nblk // 2, s // tm),
        in_specs=[pl.BlockSpec((tm, k), lambda j, i: (i, 0)),
                  pl.BlockSpec((None, k, tn), lambda j, i: (j, 0, 0)),
                  pl.BlockSpec((None, k, tn), lambda j, i: (nblk // 2 + j, 0, 0))],
        out_specs=[out, out, out],
        out_shape=[jax.ShapeDtypeStruct((s, ff), F32), jax.ShapeDtypeStruct((s, ff), F32),
                   jax.ShapeDtypeStruct((s, ff), BF16)],
        compiler_params=_params(),
    )(h2, w_blocked, w_blocked)


def _ffn_out_bwd_swiglu(dy, w_ffn_out, gate, up, name, rider=None):
    s, d = dy.shape
    ff = gate.shape[1]
    tm = 256
    nt_dims = (((1,), (1,)), ((), ()))

    def body(dy_ref, w_ref, g_ref, u_ref, o_ref):
        dv = lax.dot_general(dy_ref[...], w_ref[...], nt_dims, preferred_element_type=F32)
        gt = g_ref[...]
        sg = jax.nn.sigmoid(gt)
        o_ref[:, 0:ff] = (dv * u_ref[...] * (sg * (1.0 + gt * (1.0 - sg)))).astype(BF16)
        o_ref[:, ff:2 * ff] = (dv * gt * sg).astype(BF16)

    row = pl.BlockSpec((tm, ff), lambda i: (i, 0))
    return _pallas(
        body, name=name, grid=(s // tm,),
        in_specs=[pl.BlockSpec((tm, d), lambda i: (i, 0)), pl.BlockSpec((ff, d), lambda i: (0, 0)), row, row],
        out_specs=pl.BlockSpec((tm, 2 * ff), lambda i: (i, 0)),
        out_shape=jax.ShapeDtypeStruct((s, 2 * ff), BF16), operands=[dy, w_ffn_out, gate, up], rider=rider)


def _ffn_out_loss(act, w_ffn_out, x1, target, name):
    s, k = act.shape
    d = w_ffn_out.shape[1]
    tm = 512

    def body(a_ref, w_ref, x1_ref, t_ref, dy_ref, dyb_ref, loss_ref, acc):
        y = x1_ref[...] + jnp.dot(a_ref[...], w_ref[...], preferred_element_type=F32)
        diff = y - t_ref[...]
        dy = diff * (1.0 / d)
        dy_ref[...] = dy
        dyb_ref[...] = dy.astype(BF16)
        part = jnp.sum((diff * diff).reshape(tm // 8, 8, d), axis=0)
        i = pl.program_id(0)

        @pl.when(i == 0)
        def _():
            acc[...] = part

        @pl.when(i > 0)
        def _():
            acc[...] += part

        @pl.when(i == pl.num_programs(0) - 1)
        def _():
            loss_ref[...] = (0.5 / d) * jnp.sum(jnp.sum(acc[...], axis=1, keepdims=True), axis=0, keepdims=True)

    row = pl.BlockSpec((tm, d), lambda i: (i, 0))
    return pl.pallas_call(
        body, name=name, grid=(s // tm,),
        in_specs=[pl.BlockSpec((tm, k), lambda i: (i, 0)), pl.BlockSpec((k, d), lambda i: (0, 0)), row, row],
        out_specs=[row, row, pl.BlockSpec((1, 1), lambda i: (0, 0))],
        out_shape=[jax.ShapeDtypeStruct((s, d), F32), jax.ShapeDtypeStruct((s, d), BF16),
                   jax.ShapeDtypeStruct((1, 1), F32)],
        scratch_shapes=[pltpu.VMEM((8, d), F32)], compiler_params=_params(),
    )(act, w_ffn_out, x1, target)


LATE_GATHER = ("w_o_attn", "w_pw_conv", "w_out", "w_ffn_in", "w_ffn_out")
EARLY_REDUCE = LATE_GATHER


def _blocks_by_half(g):
    if g.ndim == 2:
        g = g.reshape(N_CHIPS, g.shape[0] // N_CHIPS, g.shape[1])
    return g.reshape(N_CHIPS, 2, g.shape[1] // 2, g.shape[2])


def _forward_backward(x, pos_col, target, wts, first_gather, late_bufs, pos_arr):
    wts = dict(wts)
    consts = _rope_consts()
    bd = consts[3]
    qw2 = jnp.tile(wts["q_norm_w"], (1, LANES // HEAD_DIM))
    kw2 = jnp.tile(wts["k_norm_w"], (1, LANES // HEAD_DIM))
    qkv_w = 3 * N_SLOT_HEADS * HEAD_DIM
    conv_col0 = 3 * qkv_w

    h = _rmsnorm_fwd(x, wts["norm1_w"], "rms1_fwd")
    slot_order = jnp.bitwise_xor(pos_arr[1], jnp.asarray([0, 2, 1, 3], jnp.int32))
    blocked = lambda buf: buf.reshape(N_CHIPS, -1, buf.shape[3])
    near = first_gather
    rope = _rope_tables(pos_col, consts, "rope_tables", after=list(late_bufs))
    proj = _proj_by_slot(h, blocked(near[2][0]), slot_order, 0, 1, "mm_proj_own", after=list(rope))
    near = _split_middle(near, after=[proj], name="allgather_w_in_near_forward")
    far, _ = _split_start(_gather_both_legs_rider(near[2][:1], near[2][1:], FAR_CHIPS), "allgather_w_in_far_start")
    _, bufs = _split_wait((near[0], near[1], far[2], near[3]), after=[], name="allgather_w_in_near_wait")
    proj = _proj_by_slot(h, blocked(bufs[0]), slot_order, 1, len(NEAR_CHIPS), "mm_proj_near", proj=proj)
    far = _split_middle((far[0], far[1], bufs, far[3]), after=[proj], name="allgather_w_in_far_forward")
    (w_in_buf, conv_w_buf, b_gate_buf), _ = _split_wait(far, after=[], name="allgather_w_in_far_wait")
    wts["w_in"] = w_in_buf.reshape(N_CHIPS, -1, w_in_buf.shape[3])
    wts["conv_w"] = conv_w_buf.transpose(1, 0, 2).reshape(CONV_WIDTH, -1)
    wts["b_gate"] = b_gate_buf.transpose(1, 0, 2).reshape(2, 1, -1)
    ch = wts["conv_w"].shape[1]
    gate_col0 = conv_col0 + 2 * ch
    n_mix = LATE_GATHER.index("w_ffn_in")
    mix_rider, ffn_rider = _gather_ici_rider(late_bufs[:n_mix], []), _gather_ici_rider(late_bufs[n_mix:], [])
    both, started = _split_start(_riders_together(mix_rider, ffn_rider), "late_gather_start", after=[wts["w_in"]])
    n_sem, n_buf = len(mix_rider.scratch), len(mix_rider.operands)
    mix_gather = (mix_rider, both[1][:n_sem], both[2][:n_buf], [])
    ffn_gather = (ffn_rider, both[1][n_sem:], both[2][n_buf:], [])
    proj = _proj_by_slot(h, wts["w_in"], slot_order, 1 + len(NEAR_CHIPS), len(FAR_CHIPS), "mm_proj_far", proj=proj,
                         after=[started])
    qn, kn = _qk_fwd(proj, rope, qw2, kw2, bd, "qk_fwd")
    attn, lse, attn_b = _attn_fwd(qn, kn, proj, "attn_fwd")

    def gathered(names, bufs):
        for n, buf in zip(names, bufs):
            full = buf.reshape(N_CHIPS, -1, buf.shape[3])
            wts[n] = full.reshape(-1, full.shape[2]) if n in ROW_SHARDED else full

    mix_bufs, _ = _split_wait(mix_gather, after=[attn_b], name="late_gather_mix_wait")
    (u0, uc), mix_bufs = _conv_fwd(proj, wts["conv_w"], wts["conv_b"], conv_col0, "conv_fwd",
                                   rider=_gather_forward_rider(mix_bufs))
    gathered(LATE_GATHER[:n_mix], mix_bufs)
    u3 = _ln_silu_fwd(uc, wts["conv_ln_w"], wts["conv_ln_b"], "ln_fwd")
    ffn_bufs, _ = _split_wait(ffn_gather, after=[u3], name="late_gather_ffn_wait")
    (y_a, y_b, mixed, x1, h2), ffn_bufs = _mix_out_proj(
        attn_b, u3, wts["w_o_attn"], wts["w_pw_conv"], proj, wts["b_gate"], gate_col0, wts["w_out"], x,
        wts["norm2_w"], "mix_x1_rms2", rider=_gather_forward_rider(ffn_bufs))
    gathered(LATE_GATHER[n_mix:], ffn_bufs)
    gate, up, act = _ffn_in_swiglu(h2, wts["w_ffn_in"], "mm_gu_swiglu")
    dy, dy_b16, loss = _ffn_out_loss(act, wts["w_ffn_out"], x1, target, "mm_x2_loss")

    g = {}
    by_chip = {}

    def pair_add(n, blocks, received):
        return _add_own_half(blocks, received, pos_arr, f"grads_pair_add_{n}")

    g_ffn_out = _blocks_by_half(
        _matmul(act, dy_b16, mode="tn", tm=1408, tn=1024, tk=2048, out_dtype=F32, name="mm_dwffnout"))
    dgu, (received,) = _ffn_out_bwd_swiglu(dy_b16, wts["w_ffn_out"], gate, up, "mm_dact_swiglu_bwd",
                                           rider=_pair_exchange_rider([g_ffn_out], halved=True))
    to_send, own = pair_add("w_ffn_out", g_ffn_out, received)
    (dx1, dx1_b16, g["norm2_w"]), (by_chip["w_ffn_out"],) = _matmul_nt_rmsnorm_bwd(
        dgu, wts["w_ffn_in"], x1, wts["norm2_w"], dy, "mm_dh2_rms2_bwd", rider=_chip_exchange_rider([to_send], [own]))
    g_ffn_in = _blocks_by_half(_matmul(h2, dgu, mode="tn", tm=512, tn=1408, tk=2048, out_dtype=F32,
                                       name="mm_dwffnin", out_blocked=N_CHIPS, cols_outer=True))
    exchanging, started = _split_start(_pair_exchange_rider([g_ffn_in], halved=True), "grads_ffn_in_pair_start")
    g["w_out"] = _matmul(mixed, dx1_b16, mode="tn", tm=512, tn=1024, tk=2048, out_dtype=F32, name="mm_dwout",
                         after=[started])
    dgl, dy_a, dy_b, g["b_gate"], dattn, du3 = _out_proj_bwd_gates(
        dx1_b16, wts["w_out"], proj, wts["b_gate"], y_a, y_b, wts["w_o_attn"], wts["w_pw_conv"], gate_col0,
        "mix_bwd")
    (received,), (g_ffn_in,) = _split_wait(exchanging, after=[dgl], name="grads_ffn_in_pair_wait")
    ffn_in_to_send, ffn_in_own = pair_add("w_ffn_in", g_ffn_in, received)
    g["w_o_attn"] = _matmul(attn_b, dy_a, mode="tn", tm=512, tn=256, tk=2048, out_dtype=F32, name="mm_dwo",
                            out_blocked=N_CHIPS)
    g["w_pw_conv"] = _matmul(u3, dy_b, mode="tn", tm=512, tn=256, tk=2048, out_dtype=F32, name="mm_dwpw",
                             out_blocked=N_CHIPS)
    duc, g["conv_ln_w"], g["conv_ln_b"] = _ln_silu_bwd(du3, uc, wts["conv_ln_w"], wts["conv_ln_b"], "ln_bwd")

    small3 = ("w_out", "w_o_attn", "w_pw_conv")
    g_small3 = [_blocks_by_half(g.pop(n)) for n in small3]
    (da, db, g["conv_w"], g["conv_b"]), received = _conv_bwd(
        duc, u0, proj, wts["conv_w"], conv_col0, "conv_bwd", rider=_pair_exchange_rider(g_small3, halved=True))
    sums3 = [pair_add(n, gb, rv) for n, gb, rv in zip(small3, g_small3, received)]
    (dqn, dkn, dv), (by_chip["w_ffn_in"],) = _attn_bwd(
        qn, kn, proj, dattn, attn, lse, bd, "attn_bwd",
        rider=_chip_exchange_rider([ffn_in_to_send], [ffn_in_own]))
    (dproj, dqw, dkw), exchanged3 = _qk_bwd(
        dqn, dkn, dv, da, db, dgl, proj, rope, qw2, kw2, bd, "qk_bwd",
        rider=_chip_exchange_rider([s[0] for s in sums3], [s[1] for s in sums3]))
    by_chip.update(zip(small3, exchanged3))
    halves = [_sum_chips(by_chip[n], pos_arr, f"grads_chip_sum_{n}") for n in EARLY_REDUCE]
    g["q_norm_w"] = dqw[:, :HEAD_DIM]
    g["k_norm_w"] = dkw[:, :HEAD_DIM]

    c = pos_arr[0]
    rh = h.shape[1] // 2
    h_sibling = lax.dynamic_slice_in_dim(h, (1 - c) * rh, rh, axis=1)
    h_own = lax.dynamic_slice_in_dim(h, c * rh, rh, axis=1)
    g_sibling, shards = _matmul(h_sibling, dproj, mode="tn", tm=rh, tn=1920, tk=2048, out_dtype=F32,
                                name="mm_dwin_sibling", out_blocked=N_CHIPS, rider=_pair_gather_rider(halves))
    reduced = dict(zip(EARLY_REDUCE, shards))
    exchanging, started = _split_start(_pair_exchange_rider([g_sibling], halved=False), "grads_w_in_pair_start")
    g_own = _matmul(h_own, dproj, mode="tn", tm=rh, tn=1920, tk=2048, out_dtype=F32, name="mm_dwin_own",
                    out_blocked=N_CHIPS, after=[started])
    (from_sibling,), _ = _split_wait(exchanging, after=[g_own], name="grads_w_in_pair_wait")
    to_send, own = _add_own_half(g_own, from_sibling, pos_arr, "grads_pair_add_w_in")
    in_flight, started = _split_start(_chip_exchange_rider([to_send], [own]), "grads_w_in_exchange_start")
    grad_x, _, g["norm1_w"] = _matmul_nt_rmsnorm_bwd(dproj, wts["w_in"], x, wts["norm1_w"], dx1, "mm_dh_rms1_bwd",
                                                     after=[started])
    return loss, grad_x, g, reduced, (in_flight, started)


def _mesh_pos():
    return lax.axis_index("x"), lax.axis_index("y"), lax.axis_index("c")


def _other_chips(x, y):
    return [(1 - x, y), (x, 1 - y), (1 - x, 1 - y)]


NEAR_CHIPS, FAR_CHIPS = (0, 1), (2,)


def _cast_into_slot(shard, chip_arr, dtype, name, n_slots=N_CHIPS, after=()):
    r, c = shard.shape
    tr = r // 2 if r % 32 == 0 else r

    def body(chip_ref, s_ref, *refs):
        refs[-1][...] = s_ref[...].astype(dtype)

    return pl.pallas_call(
        body, name=name,
        grid_spec=pltpu.PrefetchScalarGridSpec(
            num_scalar_prefetch=1, grid=(r // tr,),
            in_specs=[pl.BlockSpec((tr, c), lambda i, chip_ref: (i, 0))] + [ANY] * len(after),
            out_specs=pl.BlockSpec((None, tr, c), lambda i, chip_ref: (chip_ref[0], i, 0))),
        out_shape=jax.ShapeDtypeStruct((n_slots, r, c), dtype), compiler_params=_params(),
    )(chip_arr, shard, *after)


GATHER_CHUNKS = 8


def _gather_both_legs_rider(big, small, peers):
    nb = len(big)
    n = nb + len(small)
    nch = GATHER_CHUNKS

    def part(bufs, a, slot, half, ch):
        if a >= nb:
            return bufs[a].at[slot]
        rows = bufs[a].shape[2] // nch
        return bufs[a].at[slot, half, pl.ds(ch * rows, rows)]

    def pieces():
        return [(a, ch, k) for ch in range(nch) for a in range(n) for k in peers if a < nb or ch == 0]

    def ici(bufs, sems, a, ch, k, slot_of_src):
        x, y, c = _mesh_pos()
        px, py = _other_chips(x, y)[k]
        slot = 2 * x + y if slot_of_src == "mine" else 2 * px + py
        return pltpu.make_async_remote_copy(
            src_ref=part(bufs, a, slot, c, ch), dst_ref=part(bufs, a, slot, c, ch), send_sem=sems[0].at[a, ch, k],
            recv_sem=sems[1].at[a, ch, k], device_id=(px, py, c), device_id_type=MESH)

    def forward(bufs, sems, a, ch, k, half):
        x, y, c = _mesh_pos()
        px, py = _other_chips(x, y)[k]
        h = c if half == "mine" else 1 - c
        return pltpu.make_async_remote_copy(
            src_ref=part(bufs, a, 2 * px + py, h, ch), dst_ref=part(bufs, a, 2 * px + py, h, ch),
            send_sem=sems[2].at[a, ch, k], recv_sem=sems[3].at[a, ch, k], device_id=(x, y, 1 - c),
            device_id_type=MESH)

    def start(r_in, bufs, sems):
        for a, ch, k in pieces():
            ici(bufs, sems, a, ch, k, "mine").start()

    def middle(r_in, bufs, sems):
        for a, ch, k in pieces():
            ici(bufs, sems, a, ch, k, "theirs").wait_recv()
            if a < nb:
                forward(bufs, sems, a, ch, k, "mine").start()
        for a, ch, k in pieces():
            ici(bufs, sems, a, ch, k, "mine").wait_send()

    def wait(r_in, bufs, sems):
        for a, ch, k in pieces():
            if a < nb:
                forward(bufs, sems, a, ch, k, "theirs").wait_recv()
        for a, ch, k in pieces():
            if a < nb:
                forward(bufs, sems, a, ch, k, "mine").wait_send()

    ops = list(big) + list(small)
    return _Rider(ops, [jax.ShapeDtypeStruct(o.shape, o.dtype) for o in ops], {i: i for i in range(n)},
                  [pltpu.SemaphoreType.DMA((n, nch, 3)), pltpu.SemaphoreType.DMA((n, nch, 3)),
                   pltpu.SemaphoreType.DMA((nb, nch, 3)), pltpu.SemaphoreType.DMA((nb, nch, 3))],
                  start, wait, middle)


def _comm_call(rider, name):
    def body():
        pass

    return _pallas(body, name=name, grid=(1,), in_specs=[], out_specs=[], out_shape=[], operands=[],
                   rider=rider)[1]


def _gather_ici_rider(big, small):
    nb = len(big)
    n = nb + len(small)

    def copies(bufs, sems):
        x, y, c = _mesh_pos()
        me = 2 * x + y
        part = lambda a, slot: bufs[a].at[slot, c] if a < nb else bufs[a].at[slot]
        out = []
        for a in range(n):
            for k, (px, py) in enumerate(_other_chips(x, y)):
                send = functools.partial(
                    pltpu.make_async_remote_copy,
                    src_ref=part(a, me), dst_ref=part(a, me), send_sem=sems[0].at[a, k],
                    recv_sem=sems[1].at[a, k], device_id=(px, py, c), device_id_type=MESH)
                recv = functools.partial(
                    pltpu.make_async_remote_copy,
                    src_ref=part(a, 2 * px + py), dst_ref=part(a, 2 * px + py), send_sem=sems[0].at[a, k],
                    recv_sem=sems[1].at[a, k], device_id=(px, py, c), device_id_type=MESH)
                out.append((send, recv))
        return out

    def start(r_in, r_out, sems):
        for send, _ in copies(r_out, sems):
            send().start()

    def wait(r_in, r_out, sems):
        cps = copies(r_out, sems)
        for _, recv in cps:
            recv().wait_recv()
        for send, _ in cps:
            send().wait_send()

    ops = list(big) + list(small)
    return _Rider(ops, [jax.ShapeDtypeStruct(o.shape, o.dtype) for o in ops], {i: i for i in range(n)},
                  [pltpu.SemaphoreType.DMA((n, 3)), pltpu.SemaphoreType.DMA((n, 3))], start, wait)


def _gather_forward_rider(big):
    n = len(big)

    def copies(bufs, sems):
        x, y, c = _mesh_pos()
        out = []
        for a in range(n):
            for k, (px, py) in enumerate(_other_chips(x, y)):
                slot = 2 * px + py
                send = functools.partial(
                    pltpu.make_async_remote_copy,
                    src_ref=bufs[a].at[slot, c], dst_ref=bufs[a].at[slot, c], send_sem=sems[0].at[a, k],
                    recv_sem=sems[1].at[a, k], device_id=(x, y, 1 - c), device_id_type=MESH)
                recv = functools.partial(
                    pltpu.make_async_remote_copy,
                    src_ref=bufs[a].at[slot, 1 - c], dst_ref=bufs[a].at[slot, 1 - c], send_sem=sems[0].at[a, k],
                    recv_sem=sems[1].at[a, k], device_id=(x, y, 1 - c), device_id_type=MESH)
                out.append((send, recv))
        return out

    def start(r_in, r_out, sems):
        for send, _ in copies(r_out, sems):
            send().start()

    def wait(r_in, r_out, sems):
        cps = copies(r_out, sems)
        for _, recv in cps:
            recv().wait_recv()
        for send, _ in cps:
            send().wait_send()

    return _Rider(big, [jax.ShapeDtypeStruct(o.shape, o.dtype) for o in big], {i: i for i in range(n)},
                  [pltpu.SemaphoreType.DMA((n, 3)), pltpu.SemaphoreType.DMA((n, 3))], start, wait)


def _pair_exchange_rider(gs, halved):
    n = len(gs)

    def copies(r_in, r_out, sems):
        x, y, c = _mesh_pos()
        return [pltpu.make_async_remote_copy(
            src_ref=r_in[a].at[:, 1 - c] if halved else r_in[a], dst_ref=r_out[a], send_sem=sems[0].at[a],
            recv_sem=sems[1].at[a], device_id=(x, y, 1 - c), device_id_type=MESH) for a in range(n)]

    def start(r_in, r_out, sems):
        for cp in copies(r_in, r_out, sems):
            cp.start()

    def wait(r_in, r_out, sems):
        for cp in copies(r_in, r_out, sems):
            cp.wait()

    return _Rider(gs, [jax.ShapeDtypeStruct((g.shape[0],) + g.shape[-2:], g.dtype) for g in gs], {},
                  [pltpu.SemaphoreType.DMA((n,)), pltpu.SemaphoreType.DMA((n,))], start, wait)


def _chip_exchange_rider(to_send, by_chip, row_range=None):
    n = len(to_send)

    def copies(r_in, r_out, sems):
        x, y, c = _mesh_pos()
        me = 2 * x + y
        rows = (lambda ref: ref) if row_range is None else (lambda ref: ref.at[pl.ds(*row_range)])
        out = []
        for a in range(n):
            for k, (px, py) in enumerate(_other_chips(x, y)):
                send = functools.partial(
                    pltpu.make_async_remote_copy,
                    src_ref=rows(r_in[a].at[2 * px + py]), dst_ref=rows(r_out[a].at[me]),
                    send_sem=sems[0].at[a, k], recv_sem=sems[1].at[a, k], device_id=(px, py, c),
                    device_id_type=MESH)
                recv = functools.partial(
                    pltpu.make_async_remote_copy,
                    src_ref=rows(r_in[a].at[me]), dst_ref=rows(r_out[a].at[2 * px + py]),
                    send_sem=sems[0].at[a, k], recv_sem=sems[1].at[a, k], device_id=(px, py, c),
                    device_id_type=MESH)
                out.append((send, recv))
        return out

    def start(r_in, r_out, sems):
        for send, _ in copies(r_in, r_out, sems):
            send().start()

    def wait(r_in, r_out, sems):
        cps = copies(r_in, r_out, sems)
        for _, recv in cps:
            recv().wait_recv()
        for send, _ in cps:
            send().wait_send()

    return _Rider(list(to_send) + list(by_chip), [jax.ShapeDtypeStruct(b.shape, b.dtype) for b in by_chip],
                  {n + i: i for i in range(n)},
                  [pltpu.SemaphoreType.DMA((n, 3)), pltpu.SemaphoreType.DMA((n, 3))], start, wait)


HBM = pl.BlockSpec(memory_space=pltpu.HBM)
SEM = pl.BlockSpec(memory_space=pltpu.SEMAPHORE)


_IN_FLIGHT = pltpu.CompilerParams(has_side_effects=pltpu.SideEffectType.DATAFLOW_SIDE_EFFECTING)


class _FlatSems:
    def __init__(self, ref, shape):
        self.ref, self.shape = ref, shape

    @property
    def at(self):
        return self

    def __getitem__(self, idx):
        idx = idx if isinstance(idx, tuple) else (idx,)
        flat = 0
        for i, n in zip(idx, self.shape):
            flat = flat * n + i
        return self.ref.at[flat]


def _flat_sem_types(rider):
    return tuple(pltpu.SemaphoreType.DMA((int(np.prod(s.shape)),)) for s in rider.scratch)


def _as_rider_sems(rider, refs):
    return [_FlatSems(r, s.shape) for r, s in zip(refs, rider.scratch)]


def _split_start(rider, name, after=()):
    n_in, n_out, n_sem = len(rider.operands), len(rider.out_shapes), len(rider.scratch)
    n_after = len(after)
    fresh = [j for j in range(n_out) if j not in rider.aliases.values()]
    by_out = {j: i for i, j in rider.aliases.items()}

    def body(*refs):
        r_in = refs[:n_in]
        refs = refs[n_in + n_after:]
        sems = refs[:n_sem]
        thru = refs[n_sem:n_sem + n_in]
        fresh_refs = refs[n_sem + n_in:n_sem + n_in + len(fresh)]
        token = refs[-1]
        r_out = [thru[by_out[j]] if j in by_out else fresh_refs[fresh.index(j)] for j in range(n_out)]
        rider.start(r_in, r_out, _as_rider_sems(rider, sems))
        token[...] = jnp.zeros_like(token)

    res = pl.pallas_call(
        body, name=name,
        out_shape=_flat_sem_types(rider) + tuple(pltpu.HBM(o.shape, o.dtype) for o in rider.operands)
        + tuple(pltpu.HBM(rider.out_shapes[j].shape, rider.out_shapes[j].dtype) for j in fresh)
        + (jax.ShapeDtypeStruct((8, LANES), F32),),
        in_specs=(HBM,) * n_in + (ANY,) * n_after,
        out_specs=(SEM,) * n_sem + (HBM,) * (n_in + len(fresh)) + (pl.BlockSpec(memory_space=pltpu.VMEM),),
        input_output_aliases={i: n_sem + i for i in range(n_in)}, compiler_params=_IN_FLIGHT,
    )(*[pltpu.with_memory_space_constraint(o, pltpu.HBM) for o in rider.operands], *after)
    return (rider, res[:n_sem], res[n_sem:n_sem + n_in], res[n_sem + n_in:-1]), res[-1]


def _split_continue(handles, after, name, phase):
    rider, sems, thru, fresh_arrays = handles
    n_in, n_out, n_sem = len(rider.operands), len(rider.out_shapes), len(rider.scratch)
    fresh = [j for j in range(n_out) if j not in rider.aliases.values()]
    by_out = {j: i for i, j in rider.aliases.items()}
    n_data = n_in + len(fresh)

    def body(*refs):
        r_in = refs[:n_in]
        fresh_refs = refs[n_in:n_data]
        sem_refs = refs[n_data:n_data + n_sem]
        r_out = [r_in[by_out[j]] if j in by_out else fresh_refs[fresh.index(j)] for j in range(n_out)]
        phase(r_in, r_out, _as_rider_sems(rider, sem_refs))

    data = list(thru) + list(fresh_arrays)
    return pl.pallas_call(
        body, name=name, out_shape=tuple(pltpu.HBM(d.shape, d.dtype) for d in data),
        in_specs=(HBM,) * n_data + (SEM,) * n_sem + (ANY,) * len(after), out_specs=(HBM,) * n_data,
        input_output_aliases={i: i for i in range(n_data)}, compiler_params=_IN_FLIGHT,
    )(*data, *sems, *after)


def _split_middle(handles, after, name):
    rider, sems, thru, _ = handles
    res = _split_continue(handles, after, name, rider.middle)
    return rider, sems, res[:len(thru)], res[len(thru):]


def _split_wait(handles, after, name):
    rider = handles[0]
    n_in, n_out = len(rider.operands), len(rider.out_shapes)
    fresh = [j for j in range(n_out) if j not in rider.aliases.values()]
    by_out = {j: i for i, j in rider.aliases.items()}
    res = _split_continue(handles, after, name, rider.wait)
    return [res[by_out[j]] if j in by_out else res[n_in + fresh.index(j)] for j in range(n_out)], res[:n_in]


def _pair_gather_rider(bufs):
    n = len(bufs)

    def copies(r_out, sems):
        x, y, c = _mesh_pos()
        out = []
        for a in range(n):
            send = functools.partial(
                    pltpu.make_async_remote_copy,
                src_ref=r_out[a].at[c], dst_ref=r_out[a].at[c], send_sem=sems[0].at[a],
                recv_sem=sems[1].at[a], device_id=(x, y, 1 - c), device_id_type=MESH)
            recv = functools.partial(
                    pltpu.make_async_remote_copy,
                src_ref=r_out[a].at[1 - c], dst_ref=r_out[a].at[1 - c], send_sem=sems[0].at[a],
                recv_sem=sems[1].at[a], device_id=(x, y, 1 - c), device_id_type=MESH)
            out.append((send, recv))
        return out

    def start(r_in, r_out, sems):
        for send, _ in copies(r_out, sems):
            send().start()

    def wait(r_in, r_out, sems):
        cps = copies(r_out, sems)
        for _, recv in cps:
            recv().wait_recv()
        for send, _ in cps:
            send().wait_send()

    return _Rider(bufs, [jax.ShapeDtypeStruct(b.shape, b.dtype) for b in bufs], {i: i for i in range(n)},
                  [pltpu.SemaphoreType.DMA((n,)), pltpu.SemaphoreType.DMA((n,))], start, wait)


def _add_own_half(g, recv, pos_arr, name):
    nb, rh, cols = g.shape[0], g.shape[-2], g.shape[-1]

    def body(pos_ref, g_ref, r_ref, send_ref, own_ref):
        s = (g_ref[...] + r_ref[...]).astype(BF16)
        send_ref[...] = s

        @pl.when(pl.program_id(0) == pos_ref[1])
        def _():
            own_ref[...] = s

    blk = pl.BlockSpec((None, rh, cols), lambda j, pos_ref: (j, 0, 0))
    g_spec = blk if g.ndim == 3 else pl.BlockSpec((None, None, rh, cols),
                                                   lambda j, pos_ref: (j, pos_ref[0], 0, 0))
    shape = jax.ShapeDtypeStruct((nb, rh, cols), BF16)
    return pl.pallas_call(
        body, name=name,
        grid_spec=pltpu.PrefetchScalarGridSpec(
            num_scalar_prefetch=1, grid=(nb,), in_specs=[g_spec, blk],
            out_specs=[blk, pl.BlockSpec((None, rh, cols), lambda j, pos_ref: (pos_ref[1], 0, 0))]),
        out_shape=[shape, shape], compiler_params=_params(),
    )(pos_arr, g, recv)


def _sum_chips(gath, pos_arr, name):
    nb, rh, cols = gath.shape

    def body(pos_ref, a_ref, b_ref, c_ref, d_ref, o_ref):
        del pos_ref
        o_ref[...] = ((a_ref[...].astype(F32) + b_ref[...].astype(F32)) + c_ref[...].astype(F32)) \
            + d_ref[...].astype(F32)

    tr = rh // 2 if (rh // 2) % 16 == 0 else rh
    specs = [pl.BlockSpec((None, tr, cols), functools.partial(lambda i, pos_ref, j: (j, i, 0), j=j))
             for j in range(nb)]
    return pl.pallas_call(
        body, name=name,
        grid_spec=pltpu.PrefetchScalarGridSpec(
            num_scalar_prefetch=1, grid=(rh // tr,), in_specs=specs,
            out_specs=pl.BlockSpec((None, tr, cols), lambda i, pos_ref: (pos_ref[0], i, 0))),
        out_shape=jax.ShapeDtypeStruct((2, rh, cols), F32), compiler_params=_params(),
    )(pos_arr, gath, gath, gath, gath)


N_DEVICES = 8


def _small_gather_rider(buf):
    def copies(r_out, sems):
        x, y, c = _mesh_pos()
        me = 4 * x + 2 * y + c
        out = []
        for r in range(1, N_DEVICES):
            px = 1 - x if r & 4 else x
            py = 1 - y if r & 2 else y
            pc = 1 - c if r & 1 else c
            out.append(pltpu.make_async_remote_copy(
                src_ref=r_out[0].at[me], dst_ref=r_out[0].at[me], send_sem=sems[0].at[r - 1],
                recv_sem=sems[1].at[r - 1], device_id=(px, py, pc), device_id_type=MESH))
        return out

    def start(r_in, r_out, sems):
        for cp in copies(r_out, sems):
            cp.start()

    def wait(r_in, r_out, sems):
        cps = copies(r_out, sems)
        for cp in cps:
            cp.wait_recv()
        for cp in cps:
            cp.wait_send()

    return _Rider([buf], [jax.ShapeDtypeStruct(buf.shape, buf.dtype)], {0: 0},
                  [pltpu.SemaphoreType.DMA((N_DEVICES - 1,)), pltpu.SemaphoreType.DMA((N_DEVICES - 1,))],
                  start, wait)


def _sum_devices(buf, name):
    def body(b_ref, o_ref):
        acc = b_ref[0]
        for i in range(1, N_DEVICES):
            acc = acc + b_ref[i]
        o_ref[...] = acc

    return _pallas(body, name=name, grid=(1,), in_specs=[pl.BlockSpec(buf.shape, lambda i: (0, 0, 0))],
                   out_specs=pl.BlockSpec(buf.shape[1:], lambda i: (0, 0)),
                   out_shape=jax.ShapeDtypeStruct(buf.shape[1:], F32), operands=[buf])


def _adamw_math(w, g, m, v):
    m = ADAM_B1 * m + (1.0 - ADAM_B1) * g
    v = ADAM_B2 * v + (1.0 - ADAM_B2) * (g * g)
    m_hat = m / (1.0 - ADAM_B1 ** ADAM_STEP)
    v_hat = v / (1.0 - ADAM_B2 ** ADAM_STEP)
    delta = -ADAM_LR * (m_hat / (jnp.sqrt(v_hat) + ADAM_EPS) + ADAM_WD * w)
    return delta, m, v


def _adamw(w, g, m, v, name, after=()):
    r, c = w.shape
    tr = next(t for t in (256, 352, 128, 64) if r % t == 0 and r >= 2 * t)

    def body(w_ref, g_ref, m_ref, v_ref, go_ref, d_ref, mo_ref, vo_ref):
        gv = g_ref[...]
        d, mn, vn = _adamw_math(w_ref[...], gv, m_ref[...], v_ref[...])
        go_ref[...] = gv
        d_ref[...] = d
        mo_ref[...] = mn
        vo_ref[...] = vn

    blk = pl.BlockSpec((tr, c), lambda i: (i, 0))
    return _pallas(body, name=name, grid=(r // tr,), in_specs=[blk] * 4, out_specs=[blk] * 4,
                   out_shape=[jax.ShapeDtypeStruct((r, c), F32)] * 4, operands=[w, g, m, v], after=after)


def _adamw_small(ws, gs, ms, vs, name):
    n = len(ws)

    def body(*refs):
        w_r, g_r, m_r, v_r = refs[:n], refs[n:2 * n], refs[2 * n:3 * n], refs[3 * n:4 * n]
        d_o, m_o, v_o = refs[4 * n:5 * n], refs[5 * n:6 * n], refs[6 * n:7 * n]
        for i in range(n):
            d, mn, vn = _adamw_math(w_r[i][...], g_r[i][...], m_r[i][...], v_r[i][...])
            d_o[i][...] = d
            m_o[i][...] = mn
            v_o[i][...] = vn

    specs = [pl.BlockSpec(w.shape, lambda i: (0, 0)) for w in ws]
    shapes = [jax.ShapeDtypeStruct(w.shape, F32) for w in ws]
    outs = pl.pallas_call(
        body, name=name, grid=(1,), in_specs=specs * 4, out_specs=specs * 3, out_shape=shapes * 3,
        compiler_params=_params(),
    )(*ws, *gs, *ms, *vs)
    return outs[:n], outs[n:2 * n], outs[2 * n:]


BIG = ("w_in", "w_o_attn", "w_pw_conv", "w_out", "w_ffn_in", "w_ffn_out")
ROW_SHARDED = ("w_out", "w_ffn_out")
SMALL = ("norm1_w", "b_gate", "q_norm_w", "k_norm_w", "conv_w", "conv_b", "conv_ln_w", "conv_ln_b", "norm2_w")
ORDER = ("norm1_w", "w_in", "b_gate", "q_norm_w", "k_norm_w", "w_o_attn", "conv_w", "conv_b", "conv_ln_w",
         "conv_ln_b", "w_pw_conv", "w_out", "norm2_w", "w_ffn_in", "w_ffn_out")
PACK_TILE = 8 * LANES


def _pack_small(parts):
    rows = []
    for p in parts:
        flat = p.reshape(-1)
        pad = (-flat.shape[0]) % PACK_TILE
        rows.append(jnp.pad(flat, (0, pad)).reshape(-1, LANES))
    return jnp.concatenate(rows, axis=0)


def _unpack_small(packed, shapes):
    out, row = [], 0
    for shp in shapes:
        size = int(np.prod(shp))
        nrow = -(-size // PACK_TILE) * (PACK_TILE // LANES)
        out.append(packed[row:row + nrow].reshape(-1)[:size].reshape(shp))
        row += nrow
    return out


def kernel(x, positions, norm1_w, w_in, b_gate, q_norm_w, k_norm_w, w_o_attn, conv_w, conv_b, conv_ln_w, conv_ln_b, w_pw_conv, w_out, norm2_w, w_ffn_in, w_ffn_out, loss_target, m_norm1_w, m_w_in, m_b_gate, m_q_norm_w, m_k_norm_w, m_w_o_attn, m_conv_w, m_conv_b, m_conv_ln_w, m_conv_ln_b, m_w_pw_conv, m_w_out, m_norm2_w, m_w_ffn_in, m_w_ffn_out, v_norm1_w, v_w_in, v_b_gate, v_q_norm_w, v_k_norm_w, v_w_o_attn, v_conv_w, v_conv_b, v_conv_ln_w, v_conv_ln_b, v_w_pw_conv, v_w_out, v_norm2_w, v_w_ffn_in, v_w_ffn_out):
    w = dict(norm1_w=norm1_w, w_in=w_in, b_gate=b_gate, q_norm_w=q_norm_w, k_norm_w=k_norm_w, w_o_attn=w_o_attn,
             conv_w=conv_w, conv_b=conv_b, conv_ln_w=conv_ln_w, conv_ln_b=conv_ln_b, w_pw_conv=w_pw_conv,
             w_out=w_out, norm2_w=norm2_w, w_ffn_in=w_ffn_in, w_ffn_out=w_ffn_out)
    m = dict(norm1_w=m_norm1_w, w_in=m_w_in, b_gate=m_b_gate, q_norm_w=m_q_norm_w, k_norm_w=m_k_norm_w,
             w_o_attn=m_w_o_attn, conv_w=m_conv_w, conv_b=m_conv_b, conv_ln_w=m_conv_ln_w,
             conv_ln_b=m_conv_ln_b, w_pw_conv=m_w_pw_conv, w_out=m_w_out, norm2_w=m_norm2_w,
             w_ffn_in=m_w_ffn_in, w_ffn_out=m_w_ffn_out)
    v = dict(norm1_w=v_norm1_w, w_in=v_w_in, b_gate=v_b_gate, q_norm_w=v_q_norm_w, k_norm_w=v_k_norm_w,
             w_o_attn=v_w_o_attn, conv_w=v_conv_w, conv_b=v_conv_b, conv_ln_w=v_conv_ln_w,
             conv_ln_b=v_conv_ln_b, w_pw_conv=v_w_pw_conv, w_out=v_w_out, norm2_w=v_norm2_w,
             w_ffn_in=v_w_ffn_in, w_ffn_out=v_w_ffn_out)
    cx, cy, cc = _mesh_pos()
    chip = 2 * cx + cy

    chip_arr = chip.reshape(1).astype(jnp.int32)
    pos_arr = jnp.stack([cc, chip]).astype(jnp.int32)
    halves = lambda buf: buf.reshape(N_CHIPS, 2, buf.shape[1] // 2, buf.shape[2])
    w_in_buf = halves(_cast_into_slot(w["w_in"][0], chip_arr, BF16, "cast_w_in"))
    small_bufs = [_cast_into_slot(w[n][0], chip_arr, F32, f"slot_{n}") for n in ("conv_w", "b_gate")]
    first_gather, started = _split_start(_gather_both_legs_rider([w_in_buf], small_bufs, NEAR_CHIPS),
                                         "allgather_w_in_near_start")
    late_bufs = [halves(_cast_into_slot(w[n][0], chip_arr, BF16, f"cast_{n}", after=[started]))
                 for n in LATE_GATHER]
    wts = dict(norm1_w=norm1_w, q_norm_w=q_norm_w, k_norm_w=k_norm_w, conv_b=conv_b, conv_ln_w=conv_ln_w,
               conv_ln_b=conv_ln_b, norm2_w=norm2_w)

    loss, grad_x, g, reduced, w_in_in_flight = _forward_backward(
        x[0], positions.reshape(-1, 1), loss_target[0], wts, first_gather, late_bufs, pos_arr)
    grads = {n: b.reshape(-1, b.shape[2]) for n, b in reduced.items()}

    w_in_in_flight, started = w_in_in_flight
    delta, new_m, new_v = {}, {}, {}
    for n in EARLY_REDUCE:
        grads[n], delta[n], new_m[n], new_v[n] = _adamw(w[n][0], grads[n], m[n][0], v[n][0], f"adamw_{n}",
                                                        after=[started])
    small_parts = [loss] + [g[n] for n in SMALL]
    small_shapes = [p.shape for p in small_parts]
    device_arr = (4 * cx + 2 * cy + cc).reshape(1).astype(jnp.int32)
    small_buf = _cast_into_slot(_pack_small(small_parts), device_arr, F32, "slot_small", n_slots=N_DEVICES)

    (by_chip_w_in,), _ = _split_wait(w_in_in_flight, after=[delta[n] for n in EARLY_REDUCE] + [small_buf],
                                     name="grads_w_in_exchange_wait")
    half_w_in = _sum_chips(by_chip_w_in, pos_arr, "grads_chip_sum_w_in")
    shard_w_in, small_buf = _comm_call(
        _riders_together(_pair_gather_rider([half_w_in]), _small_gather_rider(small_buf)),
        "grads_pair_gather_w_in_small_gather")
    summed = _sum_devices(small_buf, "small_sum")
    reduced = _unpack_small(summed, small_shapes)
    loss_total = reduced[0].reshape(())
    for n, r in zip(SMALL, reduced[1:]):
        grads[n] = r
    ch_shard = conv_w.shape[2]
    grads["conv_w"] = lax.dynamic_slice_in_dim(grads["conv_w"], chip * ch_shard, ch_shard, axis=1)
    d_shard = b_gate.shape[2]
    grads["b_gate"] = lax.dynamic_slice_in_dim(grads["b_gate"], chip * d_shard, d_shard, axis=1)

    grads["w_in"], delta["w_in"], new_m["w_in"], new_v["w_in"] = _adamw(
        w["w_in"][0], shard_w_in.reshape(-1, shard_w_in.shape[2]), m["w_in"][0], v["w_in"][0], "adamw_w_in")
    flat2 = lambda a: a.reshape(-1, a.shape[-1])
    d_s, m_s, v_s = _adamw_small([flat2(w[n]) for n in SMALL], [flat2(grads[n]) for n in SMALL],
                                 [flat2(m[n]) for n in SMALL], [flat2(v[n]) for n in SMALL], "adamw_small")
    for i, n in enumerate(SMALL):
        delta[n], new_m[n], new_v[n] = d_s[i], m_s[i], v_s[i]

    shaped = lambda d, n: d[n].reshape(w[n].shape)
    return (loss_total, grad_x[None], *[shaped(grads, n) for n in ORDER], *[shaped(delta, n) for n in ORDER],
            *[shaped(new_m, n) for n in ORDER], *[shaped(new_v, n) for n in ORDER])
```

```python
import functools

import numpy as np
import jax
import jax.numpy as jnp
from jax import lax
from jax.experimental import pallas as pl
from jax.experimental.pallas import tpu as pltpu

F32 = jnp.float32
BF16 = jnp.bfloat16
MESH = pl.DeviceIdType.MESH
ANY = pl.BlockSpec(memory_space=pl.ANY)

HEAD_DIM = 64
N_SLOT_HEADS = 8
DILATIONS = (1, 4, 16)
HALF_SPAN = 64
ROPE_THETA = 500000.0
ROT_DIM = 16
CONV_WIDTH = 31
EPS = 1e-6
NEG_INF = -1e30
ADAM_LR, ADAM_B1, ADAM_B2, ADAM_EPS, ADAM_WD, ADAM_STEP = 0.001, 0.9, 0.999, 1e-08, 0.01, 10

LANES = 128
QBLK = 128
KWIN = QBLK + 2 * HALF_SPAN
VMEM_LIMIT = 48 * 1024 * 1024
N_CHIPS = 4


def _params(**kw):
    return pltpu.CompilerParams(vmem_limit_bytes=VMEM_LIMIT, **kw)


class _Rider:
    def __init__(self, operands, out_shapes, aliases, scratch, start, wait, middle=None):
        self.operands, self.out_shapes, self.aliases = list(operands), list(out_shapes), dict(aliases)
        self.scratch, self.start, self.wait = list(scratch), start, wait
        self.middle = middle


def _riders_together(a, b):
    n_in, n_out, n_sc = len(a.operands), len(a.out_shapes), len(a.scratch)
    aliases = dict(a.aliases)
    aliases.update({n_in + src: n_out + dst for src, dst in b.aliases.items()})

    def start(r_in, r_out, r_sc):
        a.start(r_in[:n_in], r_out[:n_out], r_sc[:n_sc])
        b.start(r_in[n_in:], r_out[n_out:], r_sc[n_sc:])

    def wait(r_in, r_out, r_sc):
        a.wait(r_in[:n_in], r_out[:n_out], r_sc[:n_sc])
        b.wait(r_in[n_in:], r_out[n_out:], r_sc[n_sc:])

    return _Rider(a.operands + b.operands, a.out_shapes + b.out_shapes, aliases, a.scratch + b.scratch, start, wait)


def _pallas(body, *, name, grid, in_specs, out_specs, out_shape, operands, scratch_shapes=(), aliases=None,
            rider=None, after=()):
    single = not isinstance(out_specs, (list, tuple))
    out_specs_l = [out_specs] if single else list(out_specs)
    out_shape_l = [out_shape] if single else list(out_shape)
    aliases = dict(aliases or {})

    def call(fn, all_in_specs, all_out_specs, all_out_shape, all_scratch, all_aliases, all_operands):
        return pl.pallas_call(
            fn, name=name, grid=grid, in_specs=all_in_specs, out_specs=all_out_specs, out_shape=all_out_shape,
            scratch_shapes=all_scratch, input_output_aliases=all_aliases, compiler_params=_params(),
        )(*all_operands)

    if rider is None:
        n_main = len(in_specs)

        def ordered(*refs):
            body(*refs[:n_main], *refs[n_main + len(after):])

        res = call(ordered if after else body, list(in_specs) + [ANY] * len(after), out_specs_l, out_shape_l,
                   list(scratch_shapes), aliases, list(operands) + list(after))
        return res[0] if single else res
    assert not after
    n_in, n_rin = len(in_specs), len(rider.operands)
    n_out, n_rout = len(out_specs_l), len(rider.out_shapes)
    n_sc = len(scratch_shapes)

    def wrapped(*refs):
        main_in, r_in = refs[:n_in], refs[n_in:n_in + n_rin]
        o0 = n_in + n_rin
        main_out, r_out = refs[o0:o0 + n_out], refs[o0 + n_out:o0 + n_out + n_rout]
        s0 = o0 + n_out + n_rout
        main_sc, r_sc = refs[s0:s0 + n_sc], refs[s0 + n_sc:]
        ids = [pl.program_id(d) for d in range(len(grid))]
        first = functools.reduce(jnp.logical_and, [i == 0 for i in ids])
        last = functools.reduce(jnp.logical_and, [i == n - 1 for i, n in zip(ids, grid)])

        @pl.when(first)
        def _():
            rider.start(r_in, r_out, r_sc)

        body(*main_in, *main_out, *main_sc)

        @pl.when(last)
        def _():
            rider.wait(r_in, r_out, r_sc)

    for src, dst in rider.aliases.items():
        aliases[n_in + src] = n_out + dst
    res = call(wrapped, list(in_specs) + [ANY] * n_rin, out_specs_l + [ANY] * n_rout,
               out_shape_l + rider.out_shapes, list(scratch_shapes) + rider.scratch, aliases,
               list(operands) + rider.operands)
    main = res[:n_out]
    return (main[0] if single else main), res[n_out:]


def _matmul(a, b, *, mode, tm, tn, tk, out_dtype, name, b_blocked=False,
            out_blocked=None, cols_outer=False, rider=None, after=()):
    a_shape = a.shape
    if mode == "nn":
        m_dim, k_dim = a_shape
        n_dim = b.shape[0] * b.shape[2] if b_blocked else b.shape[1]
        rows, cols, red = m_dim, n_dim, k_dim
    elif mode == "nt":
        m_dim, n_dim = a_shape
        k_dim = b.shape[1] if b_blocked else b.shape[0]
        rows, cols, red = m_dim, k_dim, n_dim
    else:
        m_dim, k_dim = a_shape
        n_dim = b.shape[1]
        rows, cols, red = k_dim, n_dim, m_dim
    assert rows % tm == 0 and cols % tn == 0 and red % tk == 0, (name, rows, cols, red)
    ni, nj, nk = rows // tm, cols // tn, red // tk

    if mode == "nn":
        a_spec = pl.BlockSpec((tm, tk), lambda i, j, k: (i, k))
        if b_blocked:
            per = b.shape[2] // tn
            b_spec = pl.BlockSpec((None, tk, tn), lambda i, j, k: (j // per, k, j % per))
        else:
            b_spec = pl.BlockSpec((tk, tn), lambda i, j, k: (k, j))
        dims = (((1,), (0,)), ((), ()))
    elif mode == "nt":
        a_spec = pl.BlockSpec((tm, tk), lambda i, j, k: (i, k))
        if b_blocked:
            per = b.shape[2] // tk
            b_spec = pl.BlockSpec((None, tn, tk), lambda i, j, k: (k // per, j, k % per))
        else:
            b_spec = pl.BlockSpec((tn, tk), lambda i, j, k: (j, k))
        dims = (((1,), (1,)), ((), ()))
    else:
        a_spec = pl.BlockSpec((tk, tm), lambda i, j, k: (k, i))
        b_spec = pl.BlockSpec((tk, tn), lambda i, j, k: (k, j))
        dims = (((0,), (0,)), ((), ()))

    if out_blocked:
        per_o = (cols // out_blocked) // tn
        out_spec = pl.BlockSpec((None, tm, tn), lambda i, j, k: (j // per_o, i, j % per_o))
        out_shape = jax.ShapeDtypeStruct((out_blocked, rows, cols // out_blocked), out_dtype)
    else:
        out_spec = pl.BlockSpec((tm, tn), lambda i, j, k: (i, j))
        out_shape = jax.ShapeDtypeStruct((rows, cols), out_dtype)

    def body(a_ref, b_ref, o_ref, *acc):
        prod = lax.dot_general(a_ref[...], b_ref[...], dims, preferred_element_type=F32)
        if nk == 1:
            o_ref[...] = prod.astype(out_dtype)
        else:
            acc_ref, = acc
            k = pl.program_id(2)

            @pl.when(k == 0)
            def _():
                acc_ref[...] = prod

            @pl.when(k > 0)
            def _():
                acc_ref[...] += prod

            @pl.when(k == nk - 1)
            def _():
                o_ref[...] = acc_ref[...].astype(out_dtype)

    scratch = [pltpu.VMEM((tm, tn), F32)] if nk > 1 else []
    grid = (ni, nj, nk)
    if cols_outer:
        swap = lambda spec: pl.BlockSpec(spec.block_shape, lambda j, i, k, f=spec.index_map: f(i, j, k))
        a_spec, b_spec, out_spec, grid = swap(a_spec), swap(b_spec), swap(out_spec), (nj, ni, nk)
    return _pallas(body, name=name, grid=grid, in_specs=[a_spec, b_spec], out_specs=out_spec,
                   out_shape=out_shape, operands=[a, b], scratch_shapes=scratch, rider=rider, after=after)


def _proj_by_slot(h, w_in, order, first, count, name, proj=None, after=()):
    s, k = h.shape
    nb = w_in.shape[2]
    tm = 512
    prev = [] if proj is None else [proj]

    def body(order_ref, h_ref, w_ref, *refs):
        refs[-1][...] = jnp.dot(h_ref[...], w_ref[...], preferred_element_type=F32)

    return pl.pallas_call(
        body, name=name,
        grid_spec=pltpu.PrefetchScalarGridSpec(
            num_scalar_prefetch=1, grid=(count, s // tm),
            in_specs=[pl.BlockSpec((tm, k), lambda j, i, order_ref: (i, 0)),
                      pl.BlockSpec((None, k, nb), lambda j, i, order_ref: (order_ref[first + j], 0, 0))]
            + [ANY] * (len(prev) + len(after)),
            out_specs=pl.BlockSpec((tm, nb), lambda j, i, order_ref: (i, order_ref[first + j]))),
        out_shape=jax.ShapeDtypeStruct((s, N_CHIPS * nb), F32),
        input_output_aliases={3: 0} if prev else {}, compiler_params=_params(),
    )(order, h, w_in, *prev, *after)


def _rmsnorm_fwd(x, w, name):
    s, d = x.shape
    tm = 256

    def body(x_ref, w_ref, o_ref):
        xv = x_ref[...]
        rstd = lax.rsqrt(jnp.mean(xv * xv, axis=-1, keepdims=True) + EPS)
        o_ref[...] = (xv * rstd * w_ref[...]).astype(BF16)

    return pl.pallas_call(
        body, name=name, grid=(s // tm,),
        in_specs=[pl.BlockSpec((tm, d), lambda i: (i, 0)), pl.BlockSpec((1, d), lambda i: (0, 0))],
        out_specs=pl.BlockSpec((tm, d), lambda i: (i, 0)),
        out_shape=jax.ShapeDtypeStruct((s, d), BF16), compiler_params=_params(),
    )(x, w)


def _matmul_nt_rmsnorm_bwd(dz, w_blocked, x, w, dres, name, bf16_copy, rider=None, after=()):
    s, d = x.shape
    nk, _, nb = w_blocked.shape
    tm = 512
    nt_dims = (((1,), (1,)), ((), ()))

    def body(a_ref, b_ref, x_ref, w_ref, dres_ref, dx_ref, *rest):
        dw_ref, acc_ref = rest[-2:]
        k, i = pl.program_id(0), pl.program_id(1)
        rows = pl.ds(pl.multiple_of(i * tm, tm), tm)
        prod = lax.dot_general(a_ref[...], b_ref[...], nt_dims, preferred_element_type=F32)

        @pl.when(k == 0)
        def _():
            acc_ref[rows, :] = prod

        @pl.when(jnp.logical_and(k > 0, k < nk - 1))
        def _():
            acc_ref[rows, :] += prod

        @pl.when(k == nk - 1)
        def _():
            xv = x_ref[...]
            rstd = lax.rsqrt(jnp.mean(xv * xv, axis=-1, keepdims=True) + EPS)
            xhat = xv * rstd
            dhv = acc_ref[rows, :] + prod
            g = dhv * w_ref[...]
            dx = rstd * (g - xhat * jnp.mean(g * xhat, axis=-1, keepdims=True)) + dres_ref[...]
            dx_ref[...] = dx
            if bf16_copy:
                rest[0][...] = dx.astype(BF16)
            part = jnp.sum(dhv * xhat, axis=0, keepdims=True)

            @pl.when(i == 0)
            def _():
                dw_ref[...] = part

            @pl.when(i > 0)
            def _():
                dw_ref[...] += part

    assert nk >= 2
    row = pl.BlockSpec((tm, d), lambda k, i: (jnp.where(k == nk - 1, i, 0), 0))
    vec = pl.BlockSpec((1, d), lambda k, i: (0, 0))
    return _pallas(
        body, name=name, grid=(nk, s // tm),
        in_specs=[pl.BlockSpec((tm, nb), lambda k, i: (i, k)), pl.BlockSpec((None, d, nb), lambda k, i: (k, 0, 0)),
                  row, vec, row],
        out_specs=[row] + [row] * bf16_copy + [vec],
        out_shape=[jax.ShapeDtypeStruct((s, d), F32)] + [jax.ShapeDtypeStruct((s, d), BF16)] * bf16_copy
        + [jax.ShapeDtypeStruct((1, d), F32)],
        operands=[dz, w_blocked, x, w, dres], scratch_shapes=[pltpu.VMEM((s, d), F32)], rider=rider, after=after)


def _rope_consts():
    lane = np.arange(LANES)
    in_head = lane % HEAD_DIM
    inv_freq = ROPE_THETA ** (-jnp.arange(0, ROT_DIM, 2, dtype=F32) / ROT_DIM)
    invf = jnp.where(jnp.asarray(in_head < ROT_DIM), jnp.tile(inv_freq, LANES // (ROT_DIM // 2)), 0.0)
    m_a = np.where(in_head < ROT_DIM // 2, -1.0, 0.0).astype(np.float32)
    m_b = np.where((in_head >= ROT_DIM // 2) & (in_head < ROT_DIM), 1.0, 0.0).astype(np.float32)
    block_diag = (lane[:, None] // HEAD_DIM == lane[None, :] // HEAD_DIM).astype(np.float32)
    return (invf.reshape(1, LANES).astype(F32), jnp.asarray(m_a).reshape(1, LANES),
            jnp.asarray(m_b).reshape(1, LANES), jnp.asarray(block_diag, dtype=BF16))


def _head_sums(v, bd):
    hi = v.astype(BF16)
    lo = (v - hi.astype(F32)).astype(BF16)
    return jnp.dot(hi, bd, preferred_element_type=F32) + jnp.dot(lo, bd, preferred_element_type=F32)


def _rope_tables(pos_col, consts, name, after=()):
    s = pos_col.shape[0]
    tm = 512
    invf, m_a, m_b, _ = consts

    def body(pos_ref, invf_ref, ma_ref, mb_ref, cos_ref, sa_ref, sb_ref):
        ang = pos_ref[...].astype(F32) * invf_ref[...]
        sin = jnp.sin(ang)
        cos_ref[...] = jnp.cos(ang)
        sa_ref[...] = sin * ma_ref[...]
        sb_ref[...] = sin * mb_ref[...]

    vec = pl.BlockSpec((1, LANES), lambda i: (0, 0))
    tab = pl.BlockSpec((tm, LANES), lambda i: (i, 0))
    return _pallas(body, name=name, grid=(s // tm,), in_specs=[pl.BlockSpec((tm, 1), lambda i: (i, 0)), vec, vec, vec],
                   out_specs=[tab] * 3, out_shape=[jax.ShapeDtypeStruct((s, LANES), F32)] * 3,
                   operands=[pos_col, invf, m_a, m_b], after=after)


def _qk_fwd(proj, rope, qw2, kw2, bd, name, rider=None):
    s = proj.shape[0]
    width = 3 * N_SLOT_HEADS * HEAD_DIM
    tm = 128
    scale = HEAD_DIM ** -0.5

    def body(q_ref, k_ref, cos_ref, sa_ref, sb_ref, qw_ref, kw_ref, bd_ref, qo_ref, ko_ref):
        cos, s_a, s_b = cos_ref[...], sa_ref[...], sb_ref[...]
        bdv = bd_ref[...]
        for src, w_ref, dst, sc in ((q_ref, qw_ref, qo_ref, scale), (k_ref, kw_ref, ko_ref, 1.0)):
            for cb in range(width // LANES):
                cols = slice(cb * LANES, (cb + 1) * LANES)
                t = src[:, cols]
                rstd = lax.rsqrt(_head_sums(t * t, bdv) * (1.0 / HEAD_DIM) + EPS)
                y = t * rstd * w_ref[...]
                r = y * cos + pltpu.roll(y, LANES - 8, axis=1) * s_a + pltpu.roll(y, 8, axis=1) * s_b
                dst[:, cols] = r * sc if sc != 1.0 else r

    vec = pl.BlockSpec((1, LANES), lambda i: (0, 0))
    tab = pl.BlockSpec((tm, LANES), lambda i: (i, 0))
    return _pallas(
        body, name=name, grid=(s // tm,),
        in_specs=[pl.BlockSpec((tm, width), lambda i: (i, 0)), pl.BlockSpec((tm, width), lambda i: (i, 1)),
                  tab, tab, tab, vec, vec, pl.BlockSpec((LANES, LANES), lambda i: (0, 0))],
        out_specs=[pl.BlockSpec((tm, width), lambda i: (i, 0))] * 2,
        out_shape=[jax.ShapeDtypeStruct((s, width), F32)] * 2,
        operands=[proj, proj, *rope, qw2, kw2, bd], rider=rider)


def _qk_bwd(dqn, dkn, dv, da, db, dgl, proj, rope, qw2, kw2, bd, name, rider=None):
    s = proj.shape[0]
    width = 3 * N_SLOT_HEADS * HEAD_DIM
    ch = da.shape[1]
    gate_w = dgl.shape[1]
    out_w = 3 * width + 2 * ch + gate_w
    assert out_w == proj.shape[1]
    tm = 128
    scale = HEAD_DIM ** -0.5

    def body(dq_ref, dk_ref, dv_ref, da_ref, db_ref, dgl_ref, q_ref, k_ref, cos_ref, sa_ref, sb_ref, qw_ref, kw_ref,
             bd_ref, out_ref, dqw_ref, dkw_ref):
        cos, s_a, s_b = cos_ref[...], sa_ref[...], sb_ref[...]
        bdv = bd_ref[...]
        first = pl.program_id(0) == 0
        for src, dsrc, w_ref, col0, dw_ref, sc in ((q_ref, dq_ref, qw_ref, 0, dqw_ref, scale),
                                                   (k_ref, dk_ref, kw_ref, width, dkw_ref, 1.0)):
            dw_acc = jnp.zeros((1, LANES), F32)
            for cb in range(width // LANES):
                cols = slice(cb * LANES, (cb + 1) * LANES)
                t = src[:, cols]
                dr = dsrc[:, cols]
                if sc != 1.0:
                    dr = dr * sc
                dy = dr * cos + pltpu.roll(dr * s_a, 8, axis=1) + pltpu.roll(dr * s_b, LANES - 8, axis=1)
                rstd = lax.rsqrt(_head_sums(t * t, bdv) * (1.0 / HEAD_DIM) + EPS)
                xhat = t * rstd
                g = dy * w_ref[...]
                dt = rstd * (g - xhat * (_head_sums(g * xhat, bdv) * (1.0 / HEAD_DIM)))
                out_ref[:, col0 + cb * LANES: col0 + (cb + 1) * LANES] = dt.astype(BF16)
                dw_acc = dw_acc + jnp.sum(dy * xhat, axis=0, keepdims=True)
            dw_acc = dw_acc + pltpu.roll(dw_acc, HEAD_DIM, axis=1)

            @pl.when(first)
            def _(dw_ref=dw_ref, dw_acc=dw_acc):
                dw_ref[...] = dw_acc

            @pl.when(jnp.logical_not(first))
            def _(dw_ref=dw_ref, dw_acc=dw_acc):
                dw_ref[...] += dw_acc
        out_ref[:, 2 * width: 3 * width] = dv_ref[...].astype(BF16)
        out_ref[:, 3 * width: 3 * width + ch] = da_ref[...]
        out_ref[:, 3 * width + ch: 3 * width + 2 * ch] = db_ref[...]
        out_ref[:, 3 * width + 2 * ch: out_w] = dgl_ref[...]

    vec = pl.BlockSpec((1, LANES), lambda i: (0, 0))
    blk = lambda c: pl.BlockSpec((tm, width), lambda i: (i, c))
    cblk = pl.BlockSpec((tm, ch), lambda i: (i, 0))
    tab = pl.BlockSpec((tm, LANES), lambda i: (i, 0))
    return _pallas(
        body, name=name, grid=(s // tm,),
        in_specs=[blk(0), blk(0), blk(0), cblk, cblk, pl.BlockSpec((tm, gate_w), lambda i: (i, 0)),
                  blk(0), blk(1), tab, tab, tab, vec, vec, pl.BlockSpec((LANES, LANES), lambda i: (0, 0))],
        out_specs=[pl.BlockSpec((tm, out_w), lambda i: (i, 0)), vec, vec],
        out_shape=[jax.ShapeDtypeStruct((s, out_w), BF16)] + [jax.ShapeDtypeStruct((1, LANES), F32)] * 2,
        operands=[dqn, dkn, dv, da, db, dgl, proj, proj, *rope, qw2, kw2, bd], rider=rider)


def _row_chunks(n_rows, fn, chunk=256):
    def step(i, c):
        fn(pl.ds(pl.multiple_of(i * chunk, chunk), chunk))
        return c
    lax.fori_loop(0, n_rows // chunk, step, 0)


def _to_residue_major(dst, src, s, d, dst_off=0, cast=None):
    seq = s // d
    for r in range(d):
        v = src[...] if d == 1 else src[pl.ds(r, seq, stride=d), :]
        dst[dst_off + r * seq: dst_off + (r + 1) * seq, :] = v if cast is None else v.astype(cast)


def _from_residue_major(dst, src, s, d, src_off=0):
    seq = s // d
    for r in range(d):
        v = src[src_off + r * seq: src_off + (r + 1) * seq, :]
        if d == 1:
            dst[...] = v
        else:
            dst[pl.ds(r, seq, stride=d), :] = v


def _band_bias():
    qi = lax.broadcasted_iota(jnp.int32, (QBLK, KWIN), 0)
    kj = lax.broadcasted_iota(jnp.int32, (QBLK, KWIN), 1)
    return jnp.where(jnp.abs(kj - HALF_SPAN - qi) <= HALF_SPAN, 0.0, NEG_INF).astype(F32)


def _range_bias(base, seq):
    kj = lax.broadcasted_iota(jnp.int32, (1, KWIN), 1)
    lo = (base & -seq) - base + HALF_SPAN
    return jnp.where((kj >= lo) & (kj < lo + seq), 0.0, NEG_INF).astype(F32)


def _skewed_blocks(n_blk, produce, consume):
    produce(0, 0)
    for b in range(n_blk):
        consume(b, b % 2)
        if b + 1 < n_blk:
            produce(b + 1, (b + 1) % 2)


def _block_base(b):
    return b * QBLK if isinstance(b, int) else pl.multiple_of(b * QBLK, QBLK)


def _attn_fwd(qn, kn, proj, name, rider=None):
    s = qn.shape[0]
    n_pairs = N_SLOT_HEADS * HEAD_DIM // LANES
    v_col0 = 2 * qn.shape[1] // LANES
    nt_dims = (((1,), (1,)), ((), ()))

    def body(q_ref, k_ref, v_ref, attn_ref, lse_ref, attn_b_ref, q_rm, k_rm, v_rm, acc_rm, m_rm, l_rm,
             acc_p, m_p, l_p, m_run, l_run, acc_run, band, s_buf, m_buf):
        g = pl.program_id(1)
        zpad = jnp.zeros((HALF_SPAN, LANES), BF16)
        k_rm[0:HALF_SPAN, :] = zpad
        k_rm[s + HALF_SPAN: s + 2 * HALF_SPAN, :] = zpad
        v_rm[0:HALF_SPAN, 0:LANES] = zpad
        v_rm[s + HALF_SPAN: s + 2 * HALF_SPAN, 0:LANES] = zpad

        def ones_rows(rows):
            v_rm[pl.ds(rows.start, rows.size), LANES:2 * LANES] = jnp.ones((rows.size, LANES), BF16)

        _row_chunks(s + 2 * HALF_SPAN, ones_rows, chunk=2 * HALF_SPAN)
        band[...] = _band_bias()
        lane = lax.broadcasted_iota(jnp.int32, (QBLK, LANES), 1)
        low = lane < HEAD_DIM
        n_blk = s // QBLK

        for gi, d in enumerate(DILATIONS):
            @pl.when(g == gi)
            def _(gi=gi, d=d):
                seq = s // d
                _to_residue_major(q_rm, q_ref, s, d, cast=BF16)
                _to_residue_major(k_rm, k_ref, s, d, dst_off=HALF_SPAN, cast=BF16)
                _to_residue_major(v_rm.at[:, 0:LANES], v_ref, s, d, dst_off=HALF_SPAN, cast=BF16)

                def scores(b, slot):
                    base = _block_base(b)
                    q = q_rm[pl.ds(base, QBLK), :]
                    zero = jnp.zeros_like(q)
                    q2 = jnp.concatenate([jnp.where(low, q, zero), jnp.where(low, zero, q)], axis=0)
                    sc = lax.dot_general(q2, k_rm[pl.ds(base, KWIN), :], nt_dims, preferred_element_type=F32)
                    bias = band[...] + _range_bias(base, seq)
                    for hh in range(2):
                        rows = slice(hh * QBLK, (hh + 1) * QBLK)
                        sh = sc[rows, :] + bias
                        s_buf[slot, rows, :] = sh
                        m_buf[slot, rows, :] = jnp.broadcast_to(jnp.max(sh, axis=-1, keepdims=True), (QBLK, LANES))

                def outputs(b, slot):
                    base = _block_base(b)
                    sv = s_buf[slot]
                    mb = m_buf[slot]
                    p = jnp.exp(jnp.concatenate([sv[:, 0:LANES] - mb, sv[:, LANES:2 * LANES] - mb], axis=1))
                    pv = jnp.dot(p.astype(BF16), v_rm[pl.ds(base, KWIN), :], preferred_element_type=F32)
                    rows = pl.ds(base, QBLK)
                    acc_rm[rows, :] = jnp.where(low, pv[0:QBLK, 0:LANES], pv[QBLK:2 * QBLK, 0:LANES])
                    l_rm[rows, :] = jnp.where(low, pv[0:QBLK, LANES:2 * LANES], pv[QBLK:2 * QBLK, LANES:2 * LANES])
                    m_rm[rows, :] = jnp.where(low, mb[0:QBLK, :], mb[QBLK:2 * QBLK, :])

                _skewed_blocks(n_blk, scores, outputs)
                if d == 1:
                    src = (acc_rm, m_rm, l_rm)
                else:
                    for dst_, src_ in ((acc_p, acc_rm), (m_p, m_rm), (l_p, l_rm)):
                        _from_residue_major(dst_, src_, s, d)
                    src = (acc_p, m_p, l_p)

                def combine(rows):
                    a_g, m_g, l_g = src[0][rows, :], src[1][rows, :], src[2][rows, :]
                    if gi == 0:
                        m_new, l_new, a_new = m_g, l_g, a_g
                    else:
                        m_old = m_run[rows, :]
                        m_new = jnp.maximum(m_old, m_g)
                        w_old = jnp.exp(m_old - m_new)
                        w_g = jnp.exp(m_g - m_new)
                        l_new = l_run[rows, :] * w_old + l_g * w_g
                        a_new = acc_run[rows, :] * w_old + a_g * w_g
                    if gi == len(DILATIONS) - 1:
                        out = a_new / l_new
                        attn_ref[rows, :] = out
                        attn_b_ref[rows, :] = out.astype(BF16)
                        lse_ref[rows, :] = m_new + jnp.log(l_new)
                    else:
                        m_run[rows, :] = m_new
                        l_run[rows, :] = l_new
                        acc_run[rows, :] = a_new

                _row_chunks(s, combine)

    qk_spec = pl.BlockSpec((s, LANES), lambda hp, g: (0, g * n_pairs + hp))
    v_spec = pl.BlockSpec((s, LANES), lambda hp, g: (0, v_col0 + g * n_pairs + hp))
    o_spec = pl.BlockSpec((s, LANES), lambda hp, g: (0, hp))
    f32buf = pltpu.VMEM((s, LANES), F32)
    return _pallas(
        body, name=name, grid=(n_pairs, len(DILATIONS)), in_specs=[qk_spec, qk_spec, v_spec],
        out_specs=[o_spec, o_spec, o_spec],
        out_shape=[jax.ShapeDtypeStruct((s, n_pairs * LANES), F32)] * 2
        + [jax.ShapeDtypeStruct((s, n_pairs * LANES), BF16)],
        operands=[qn, kn, proj],
        scratch_shapes=[pltpu.VMEM((s, LANES), BF16), pltpu.VMEM((s + 2 * HALF_SPAN, LANES), BF16),
                        pltpu.VMEM((s + 2 * HALF_SPAN, 2 * LANES), BF16)] + [f32buf] * 9
        + [pltpu.VMEM((QBLK, KWIN), F32), pltpu.VMEM((2, 2 * QBLK, KWIN), F32),
           pltpu.VMEM((2, 2 * QBLK, LANES), F32)],
        rider=rider)


def _attn_bwd(qn, kn, proj, dattn, attn, lse, bd, name, rider=None):
    s = qn.shape[0]
    n_pairs = N_SLOT_HEADS * HEAD_DIM // LANES
    v_col0 = 2 * qn.shape[1] // LANES
    nt_dims = (((1,), (1,)), ((), ()))
    tn_dims = (((0,), (0,)), ((), ()))
    spad = s + 2 * HALF_SPAN

    def body(q_ref, k_ref, v_ref, do_ref, o_ref, lse_ref, bd_ref, dq_ref, dk_ref, dv_ref,
             q_rm, k_rm, v_rm, do_rm, lse0_rm, lse1_rm, dd0_rm, dd1_rm, dq_rm, dk_rm, dv_rm,
             lse0_p, lse1_p, dd0_p, dd1_p, band, p_buf, ds_buf):
        g = pl.program_id(1)
        zpad = jnp.zeros((HALF_SPAN, LANES), BF16)
        for buf in (k_rm, v_rm):
            buf[0:HALF_SPAN, :] = zpad
            buf[s + HALF_SPAN: spad, :] = zpad
        zf = jnp.zeros((HALF_SPAN, LANES), F32)
        for buf in (dk_rm, dv_rm):
            buf[0:HALF_SPAN, :] = zf
            buf[s + HALF_SPAN: spad, :] = zf
        band[...] = _band_bias()

        def clear(rows):
            z = jnp.zeros((rows.size, LANES), F32)
            dk_rm[pl.ds(rows.start + HALF_SPAN, rows.size), :] = z
            dv_rm[pl.ds(rows.start + HALF_SPAN, rows.size), :] = z

        _row_chunks(s, clear)

        def prepare(rows):
            lo = lax.broadcasted_iota(jnp.int32, (rows.size, LANES), 1) < HEAD_DIM
            dsum = _head_sums(do_ref[rows, :] * o_ref[rows, :], bd_ref[...])
            dswap = pltpu.roll(dsum, HEAD_DIM, axis=1)
            dd0_p[rows, :] = jnp.where(lo, dsum, dswap)
            dd1_p[rows, :] = jnp.where(lo, dswap, dsum)
            lv = lse_ref[rows, :]
            lswap = pltpu.roll(lv, HEAD_DIM, axis=1)
            lse0_p[rows, :] = jnp.where(lo, lv, lswap)
            lse1_p[rows, :] = jnp.where(lo, lswap, lv)

        @pl.when(g == 0)
        def _():
            _row_chunks(s, prepare)
        lane = lax.broadcasted_iota(jnp.int32, (QBLK, LANES), 1)
        low = lane < HEAD_DIM
        n_blk = s // QBLK

        def stacked(ref, rows):
            val = ref[rows, :]
            zero = jnp.zeros_like(val)
            return jnp.concatenate([jnp.where(low, val, zero), jnp.where(low, zero, val)], axis=0)

        for gi, d in enumerate(DILATIONS):
            @pl.when(g == gi)
            def _(d=d):
                seq = s // d
                _to_residue_major(q_rm, q_ref, s, d, cast=BF16)
                _to_residue_major(k_rm, k_ref, s, d, dst_off=HALF_SPAN, cast=BF16)
                _to_residue_major(v_rm, v_ref, s, d, dst_off=HALF_SPAN, cast=BF16)
                _to_residue_major(do_rm, do_ref, s, d, cast=BF16)
                for dst_, src_ in ((lse0_rm, lse0_p), (lse1_rm, lse1_p), (dd0_rm, dd0_p), (dd1_rm, dd1_p)):
                    _to_residue_major(dst_, src_, s, d)

                def scores(b, slot):
                    base = _block_base(b)
                    rows = pl.ds(base, QBLK)
                    win = pl.ds(base, KWIN)
                    sc = lax.dot_general(stacked(q_rm, rows), k_rm[win, :], nt_dims, preferred_element_type=F32)
                    dp = lax.dot_general(stacked(do_rm, rows), v_rm[win, :], nt_dims, preferred_element_type=F32)
                    bias = band[...] + _range_bias(base, seq)
                    for hh, (lse_r, dd_r) in enumerate(((lse0_rm, dd0_rm), (lse1_rm, dd1_rm))):
                        r = slice(hh * QBLK, (hh + 1) * QBLK)
                        lse_h = lse_r[rows, :]
                        dd_h = dd_r[rows, :]
                        sh = sc[r, :] + bias
                        p = jnp.exp(jnp.concatenate([sh[:, 0:LANES] - lse_h, sh[:, LANES:KWIN] - lse_h], axis=1))
                        dph = dp[r, :]
                        ds = p * jnp.concatenate([dph[:, 0:LANES] - dd_h, dph[:, LANES:KWIN] - dd_h], axis=1)
                        p_buf[slot, r, :] = p.astype(BF16)
                        ds_buf[slot, r, :] = ds.astype(BF16)

                def grads(b, slot):
                    base = _block_base(b)
                    rows = pl.ds(base, QBLK)
                    win = pl.ds(base, KWIN)
                    p = p_buf[slot]
                    ds = ds_buf[slot]
                    dq2 = jnp.dot(ds, k_rm[win, :], preferred_element_type=F32)
                    dq_rm[rows, :] = jnp.where(low, dq2[0:QBLK, :], dq2[QBLK:2 * QBLK, :])
                    dk_rm[win, :] += lax.dot_general(ds, stacked(q_rm, rows), tn_dims, preferred_element_type=F32)
                    dv_rm[win, :] += lax.dot_general(p, stacked(do_rm, rows), tn_dims, preferred_element_type=F32)

                _skewed_blocks(n_blk, scores, grads)
                _from_residue_major(dq_ref, dq_rm, s, d)
                _from_residue_major(dk_ref, dk_rm, s, d, src_off=HALF_SPAN)
                _from_residue_major(dv_ref, dv_rm, s, d, src_off=HALF_SPAN)

    qk_spec = pl.BlockSpec((s, LANES), lambda hp, g: (0, g * n_pairs + hp))
    v_spec = pl.BlockSpec((s, LANES), lambda hp, g: (0, v_col0 + g * n_pairs + hp))
    o_spec = pl.BlockSpec((s, LANES), lambda hp, g: (0, hp))
    width = qn.shape[1]
    f32buf = pltpu.VMEM((s, LANES), F32)
    f32pad = pltpu.VMEM((spad, LANES), F32)
    return _pallas(
        body, name=name, grid=(n_pairs, len(DILATIONS)),
        in_specs=[qk_spec, qk_spec, v_spec, o_spec, o_spec, o_spec,
                  pl.BlockSpec((LANES, LANES), lambda hp, g: (0, 0))],
        out_specs=[qk_spec, qk_spec, qk_spec],
        out_shape=[jax.ShapeDtypeStruct((s, width), F32)] * 3,
        operands=[qn, kn, proj, dattn, attn, lse, bd],
        scratch_shapes=[pltpu.VMEM((s, LANES), BF16), pltpu.VMEM((spad, LANES), BF16),
                        pltpu.VMEM((spad, LANES), BF16), pltpu.VMEM((s, LANES), BF16),
                        f32buf, f32buf, f32buf, f32buf, f32buf, f32pad, f32pad,
                        f32buf, f32buf, f32buf, f32buf, pltpu.VMEM((QBLK, KWIN), F32),
                        pltpu.VMEM((2, 2 * QBLK, KWIN), BF16), pltpu.VMEM((2, 2 * QBLK, KWIN), BF16)],
        rider=rider)


CONV_PAD = 16


def _conv_fwd(proj, conv_w, conv_b, col0, name, rider=None):
    s = proj.shape[0]
    ch = conv_w.shape[1]
    nblk = ch // LANES
    a0 = col0 // LANES
    tr = 256
    shift = CONV_PAD - (CONV_WIDTH - 1) // 2

    def body(a_ref, b_ref, w_ref, bias_ref, u0_ref, uc_ref, pad):
        z = jnp.zeros((CONV_PAD, LANES), F32)
        pad[0:CONV_PAD, :] = z
        pad[s + CONV_PAD: s + 2 * CONV_PAD, :] = z

        def glu(rows):
            u0 = a_ref[rows, :] * jax.nn.sigmoid(b_ref[rows, :])
            u0_ref[rows, :] = u0
            pad[pl.ds(rows.start + CONV_PAD, rows.size), :] = u0

        _row_chunks(s, glu)
        for t in range(0, s, tr):
            acc = jnp.broadcast_to(bias_ref[...], (tr, LANES))
            for k in range(CONV_WIDTH):
                acc = acc + w_ref[k:k + 1, :] * pad[t + k + shift: t + k + shift + tr, :]
            uc_ref[t:t + tr, :] = acc

    return _pallas(
        body, name=name, grid=(nblk,),
        in_specs=[pl.BlockSpec((s, LANES), lambda c: (0, a0 + c)),
                  pl.BlockSpec((s, LANES), lambda c: (0, a0 + nblk + c)),
                  pl.BlockSpec((CONV_WIDTH, LANES), lambda c: (0, c)),
                  pl.BlockSpec((1, LANES), lambda c: (0, c))],
        out_specs=[pl.BlockSpec((s, LANES), lambda c: (0, c))] * 2,
        out_shape=[jax.ShapeDtypeStruct((s, ch), F32)] * 2, operands=[proj, proj, conv_w, conv_b],
        scratch_shapes=[pltpu.VMEM((s + 2 * CONV_PAD, LANES), F32)], rider=rider)


def _ln_silu_fwd(uc, ln_w, ln_b, name):
    s, ch = uc.shape
    tm = 256

    def body(u_ref, w_ref, b_ref, o_ref):
        u = u_ref[...]
        mu = jnp.mean(u, axis=-1, keepdims=True)
        xc = u - mu
        rstd = lax.rsqrt(jnp.mean(xc * xc, axis=-1, keepdims=True) + EPS)
        z = xc * rstd * w_ref[...] + b_ref[...]
        o_ref[...] = (z * jax.nn.sigmoid(z)).astype(BF16)

    row = pl.BlockSpec((tm, ch), lambda i: (i, 0))
    vec = pl.BlockSpec((1, ch), lambda i: (0, 0))
    return pl.pallas_call(
        body, name=name, grid=(s // tm,), in_specs=[row, vec, vec], out_specs=row,
        out_shape=jax.ShapeDtypeStruct((s, ch), BF16), compiler_params=_params(),
    )(uc, ln_w, ln_b)


def _ln_silu_bwd(du3, uc, ln_w, ln_b, name):
    s, ch = uc.shape
    tm = 256

    def body(d_ref, u_ref, w_ref, b_ref, du_ref, dw_ref, db_ref):
        u = u_ref[...]
        mu = jnp.mean(u, axis=-1, keepdims=True)
        xc = u - mu
        rstd = lax.rsqrt(jnp.mean(xc * xc, axis=-1, keepdims=True) + EPS)
        xhat = xc * rstd
        z = xhat * w_ref[...] + b_ref[...]
        sg = jax.nn.sigmoid(z)
        dz = d_ref[...] * (sg * (1.0 + z * (1.0 - sg)))
        dxh = dz * w_ref[...]
        du_ref[...] = rstd * (dxh - jnp.mean(dxh, axis=-1, keepdims=True)
                              - xhat * jnp.mean(dxh * xhat, axis=-1, keepdims=True))
        pw = jnp.sum(dz * xhat, axis=0, keepdims=True)
        pb = jnp.sum(dz, axis=0, keepdims=True)
        first = pl.program_id(0) == 0

        @pl.when(first)
        def _():
            dw_ref[...] = pw
            db_ref[...] = pb

        @pl.when(jnp.logical_not(first))
        def _():
            dw_ref[...] += pw
            db_ref[...] += pb

    row = pl.BlockSpec((tm, ch), lambda i: (i, 0))
    vec = pl.BlockSpec((1, ch), lambda i: (0, 0))
    return pl.pallas_call(
        body, name=name, grid=(s // tm,), in_specs=[row, row, vec, vec], out_specs=[row, vec, vec],
        out_shape=[jax.ShapeDtypeStruct((s, ch), F32), jax.ShapeDtypeStruct((1, ch), F32),
                   jax.ShapeDtypeStruct((1, ch), F32)],
        compiler_params=_params(),
    )(du3, uc, ln_w, ln_b)


def _conv_bwd(duc, u0, proj, conv_w, col0, name, rider=None):
    s = proj.shape[0]
    ch = conv_w.shape[1]
    nblk = ch // LANES
    a0 = col0 // LANES
    tr = 256
    half = (CONV_WIDTH - 1) // 2
    shift = CONV_PAD - half

    def body(duc_ref, u0_ref, a_ref, b_ref, w_ref, da_ref, db_ref, dw_ref, dbias_ref, pad_d, pad_u):
        z = jnp.zeros((CONV_PAD, LANES), F32)
        for buf in (pad_d, pad_u):
            buf[0:CONV_PAD, :] = z
            buf[s + CONV_PAD: s + 2 * CONV_PAD, :] = z

        def fill(rows):
            dst = pl.ds(rows.start + CONV_PAD, rows.size)
            pad_d[dst, :] = duc_ref[rows, :]
            pad_u[dst, :] = u0_ref[rows, :]

        _row_chunks(s, fill)
        dw_acc = [jnp.zeros((8, LANES), F32) for _ in range(CONV_WIDTH)]
        dbias_acc = jnp.zeros((8, LANES), F32)
        for t in range(0, s, tr):
            d_t = duc_ref[t:t + tr, :]
            dbias_acc = dbias_acc + jnp.sum(d_t.reshape(tr // 8, 8, LANES), axis=0)
            du0 = jnp.zeros((tr, LANES), F32)
            for k in range(CONV_WIDTH):
                du0 = du0 + w_ref[k:k + 1, :] * pad_d[t - k + half + CONV_PAD: t - k + half + CONV_PAD + tr, :]
                prod = d_t * pad_u[t + k + shift: t + k + shift + tr, :]
                dw_acc[k] = dw_acc[k] + jnp.sum(prod.reshape(tr // 8, 8, LANES), axis=0)
            av = a_ref[t:t + tr, :]
            sg = jax.nn.sigmoid(b_ref[t:t + tr, :])
            da_ref[t:t + tr, :] = (du0 * sg).astype(BF16)
            db_ref[t:t + tr, :] = (du0 * av * sg * (1.0 - sg)).astype(BF16)
        for k in range(CONV_WIDTH):
            dw_ref[k:k + 1, :] = jnp.sum(dw_acc[k], axis=0, keepdims=True)
        dbias_ref[...] = jnp.sum(dbias_acc, axis=0, keepdims=True)

    col = lambda off: pl.BlockSpec((s, LANES), lambda c: (0, off + c))
    return _pallas(
        body, name=name, grid=(nblk,),
        in_specs=[col(0), col(0), col(a0), col(a0 + nblk),
                  pl.BlockSpec((CONV_WIDTH, LANES), lambda c: (0, c))],
        out_specs=[col(0), col(0), pl.BlockSpec((CONV_WIDTH, LANES), lambda c: (0, c)),
                   pl.BlockSpec((1, LANES), lambda c: (0, c))],
        out_shape=[jax.ShapeDtypeStruct((s, ch), BF16)] * 2
        + [jax.ShapeDtypeStruct((CONV_WIDTH, ch), F32), jax.ShapeDtypeStruct((1, ch), F32)],
        operands=[duc, u0, proj, proj, conv_w],
        scratch_shapes=[pltpu.VMEM((s + 2 * CONV_PAD, LANES), F32)] * 2, rider=rider)


def _mix_out_proj(attn_b, u3, w_o, w_pw, proj, bg, col0, w_out, x, norm_w, name, rider=None):
    s, d = x.shape
    k = attn_b.shape[1]
    tm = 256
    half = d // 2
    assert col0 % half == 0
    c0 = col0 // half

    def body(a_ref, u_ref, wo_ref, wp_ref, a0_ref, a1_ref, b0_ref, b1_ref, bias_ref, w_ref, x_ref, nw_ref,
             ya_ref, yb_ref, mixed_ref, x1_ref, h2_ref):
        mixed = None
        for br, (src_ref, wb_ref, lo_ref, hi_ref, y_ref) in enumerate(((a_ref, wo_ref, a0_ref, a1_ref, ya_ref),
                                                                       (u_ref, wp_ref, b0_ref, b1_ref, yb_ref))):
            src = src_ref[...]
            y = jnp.concatenate([jnp.dot(src, wb_ref[j], preferred_element_type=F32) for j in range(N_CHIPS)],
                                axis=1)
            y_ref[...] = y
            logits = jnp.concatenate([lo_ref[...], hi_ref[...]], axis=1)
            part = jax.nn.sigmoid(logits + bias_ref[br]) * y
            mixed = part if mixed is None else mixed + part
        mixed = mixed.astype(BF16)
        mixed_ref[...] = mixed
        x1 = x_ref[...] + jnp.dot(mixed, w_ref[...], preferred_element_type=F32)
        x1_ref[...] = x1
        rstd = lax.rsqrt(jnp.mean(x1 * x1, axis=-1, keepdims=True) + EPS)
        h2_ref[...] = (x1 * rstd * nw_ref[...]).astype(BF16)

    row = pl.BlockSpec((tm, d), lambda i: (i, 0))
    src_row = pl.BlockSpec((tm, k), lambda i: (i, 0))
    blocks = pl.BlockSpec(w_o.shape, lambda i: (0, 0, 0))
    logit_blk = lambda j: pl.BlockSpec((tm, half), functools.partial(lambda i, j: (i, c0 + j), j=j))
    return _pallas(
        body, name=name, grid=(s // tm,),
        in_specs=[src_row, src_row, blocks, blocks, logit_blk(0), logit_blk(1), logit_blk(2), logit_blk(3),
                  pl.BlockSpec((2, 1, d), lambda i: (0, 0, 0)), pl.BlockSpec((d, d), lambda i: (0, 0)), row,
                  pl.BlockSpec((1, d), lambda i: (0, 0))],
        out_specs=[row] * 5,
        out_shape=[jax.ShapeDtypeStruct((s, d), F32), jax.ShapeDtypeStruct((s, d), F32),
                   jax.ShapeDtypeStruct((s, d), BF16), jax.ShapeDtypeStruct((s, d), F32),
                   jax.ShapeDtypeStruct((s, d), BF16)],
        operands=[attn_b, u3, w_o, w_pw, proj, proj, proj, proj, bg, w_out, x, norm_w], rider=rider)


def _out_proj_bwd_gates(dx1, w_out, proj, bg, y_a, y_b, w_o, w_pw, col0, name, after=()):
    s, d = y_a.shape
    n_blk, k, blk = w_o.shape
    tm = 256
    half = d // 2
    assert col0 % half == 0
    c0 = col0 // half
    nt_dims = (((1,), (1,)), ((), ()))

    def body(dx_ref, w_ref, a0_ref, a1_ref, b0_ref, b1_ref, bias_ref, ya_ref, yb_ref, wo_ref, wp_ref,
             dgl_ref, dya_ref, dyb_ref, db_ref, da_ref, du_ref):
        dm = lax.dot_general(dx_ref[...], w_ref[...], nt_dims, preferred_element_type=F32)
        parts = []
        for br, (lo_ref, hi_ref, y_ref, dy_ref, wb_ref, dsrc_ref) in enumerate((
                (a0_ref, a1_ref, ya_ref, dya_ref, wo_ref, da_ref), (b0_ref, b1_ref, yb_ref, dyb_ref, wp_ref, du_ref))):
            logits = jnp.concatenate([lo_ref[...], hi_ref[...]], axis=1)
            gate = jax.nn.sigmoid(logits + bias_ref[br])
            dy = (dm * gate).astype(BF16)
            dy_ref[...] = dy
            dsrc = lax.dot_general(dy[:, 0:blk], wb_ref[0], nt_dims, preferred_element_type=F32)
            for j in range(1, n_blk):
                dsrc = dsrc + lax.dot_general(dy[:, j * blk:(j + 1) * blk], wb_ref[j], nt_dims,
                                              preferred_element_type=F32)
            dsrc_ref[...] = dsrc
            dgl = dm * y_ref[...] * gate * (1.0 - gate)
            dgl_ref[:, br * d:(br + 1) * d] = dgl.astype(BF16)
            parts.append(jnp.sum(dgl, axis=0, keepdims=True))
        part = jnp.concatenate(parts, axis=0)
        first = pl.program_id(0) == 0

        @pl.when(first)
        def _():
            db_ref[...] = part

        @pl.when(jnp.logical_not(first))
        def _():
            db_ref[...] += part

    row = pl.BlockSpec((tm, d), lambda i: (i, 0))
    logit_blk = lambda k: pl.BlockSpec((tm, half), functools.partial(lambda i, k: (i, c0 + k), k=k))
    blocks = pl.BlockSpec(w_o.shape, lambda i: (0, 0, 0))
    src_row = pl.BlockSpec((tm, k), lambda i: (i, 0))
    return _pallas(
        body, name=name, grid=(s // tm,),
        in_specs=[row, pl.BlockSpec((d, d), lambda i: (0, 0)), logit_blk(0), logit_blk(1), logit_blk(2),
                  logit_blk(3), pl.BlockSpec((2, 1, d), lambda i: (0, 0, 0)), row, row, blocks, blocks],
        out_specs=[pl.BlockSpec((tm, 2 * d), lambda i: (i, 0)), row, row, pl.BlockSpec((2, d), lambda i: (0, 0)),
                   src_row, src_row],
        out_shape=[jax.ShapeDtypeStruct((s, 2 * d), BF16), jax.ShapeDtypeStruct((s, d), BF16),
                   jax.ShapeDtypeStruct((s, d), BF16), jax.ShapeDtypeStruct((2, d), F32),
                   jax.ShapeDtypeStruct((s, k), F32), jax.ShapeDtypeStruct((s, k), F32)],
        operands=[dx1, w_out, proj, proj, proj, proj, bg, y_a, y_b, w_o, w_pw], after=after)


def _ffn_in_swiglu(h2, w_blocked, name):
    s, k = h2.shape
    nblk, _, tn = w_blocked.shape
    ff = nblk // 2 * tn
    tm = 512

    def body(a_ref, wg_ref, wu_ref, g_ref, u_ref, act_ref):
        a = a_ref[...]
        gt = jnp.dot(a, wg_ref[...], preferred_element_type=F32)
        up = jnp.dot(a, wu_ref[...], preferred_element_type=F32)
        g_ref[...] = gt
        u_ref[...] = up
        act_ref[...] = (gt * jax.nn.sigmoid(gt) * up).astype(BF16)

    out = pl.BlockSpec((tm, tn), lambda j, i: (i, j))
    return pl.pallas_call(
        body, name=name, grid=(nblk // 2, s // tm),
        in_specs=[pl.BlockSpec((tm, k), lambda j, i: (i, 0)),
                  pl.BlockSpec((None, k, tn), lambda j, i: (j, 0, 0)),
                  pl.BlockSpec((None, k, tn), lambda j, i: (nblk // 2 + j, 0, 0))],
        out_specs=[out, out, out],
        out_shape=[jax.ShapeDtypeStruct((s, ff), F32), jax.ShapeDtypeStruct((s, ff), F32),
                   jax.ShapeDtypeStruct((s, ff), BF16)],
        compiler_params=_params(),
    )(h2, w_blocked, w_blocked)


def _ffn_out_bwd_swiglu(dy, w_ffn_out, gate, up, name, rider=None):
    s, d = dy.shape
    ff = gate.shape[1]
    tm = 256
    nt_dims = (((1,), (1,)), ((), ()))

    def body(dy_ref, w_ref, g_ref, u_ref, o_ref):
        dv = lax.dot_general(dy_ref[...], w_ref[...], nt_dims, preferred_element_type=F32)
        gt = g_ref[...]
        sg = jax.nn.sigmoid(gt)
        o_ref[:, 0:ff] = (dv * u_ref[...] * (sg * (1.0 + gt * (1.0 - sg)))).astype(BF16)
        o_ref[:, ff:2 * ff] = (dv * gt * sg).astype(BF16)

    row = pl.BlockSpec((tm, ff), lambda i: (i, 0))
    return _pallas(
        body, name=name, grid=(s // tm,),
        in_specs=[pl.BlockSpec((tm, d), lambda i: (i, 0)), pl.BlockSpec((ff, d), lambda i: (0, 0)), row, row],
        out_specs=pl.BlockSpec((tm, 2 * ff), lambda i: (i, 0)),
        out_shape=jax.ShapeDtypeStruct((s, 2 * ff), BF16), operands=[dy, w_ffn_out, gate, up], rider=rider)


def _ffn_out_loss(act, w_ffn_out, x1, target, name):
    s, k = act.shape
    d = w_ffn_out.shape[1]
    tm = 512

    def body(a_ref, w_ref, x1_ref, t_ref, dy_ref, dyb_ref, loss_ref, acc):
        y = x1_ref[...] + jnp.dot(a_ref[...], w_ref[...], preferred_element_type=F32)
        diff = y - t_ref[...]
        dy = diff * (1.0 / d)
        dy_ref[...] = dy
        dyb_ref[...] = dy.astype(BF16)
        part = jnp.sum((diff * diff).reshape(tm // 8, 8, d), axis=0)
        i = pl.program_id(0)

        @pl.when(i == 0)
        def _():
            acc[...] = part

        @pl.when(i > 0)
        def _():
            acc[...] += part

        @pl.when(i == pl.num_programs(0) - 1)
        def _():
            loss_ref[...] = (0.5 / d) * jnp.sum(jnp.sum(acc[...], axis=1, keepdims=True), axis=0, keepdims=True)

    row = pl.BlockSpec((tm, d), lambda i: (i, 0))
    return pl.pallas_call(
        body, name=name, grid=(s // tm,),
        in_specs=[pl.BlockSpec((tm, k), lambda i: (i, 0)), pl.BlockSpec((k, d), lambda i: (0, 0)), row, row],
        out_specs=[row, row, pl.BlockSpec((1, 1), lambda i: (0, 0))],
        out_shape=[jax.ShapeDtypeStruct((s, d), F32), jax.ShapeDtypeStruct((s, d), BF16),
                   jax.ShapeDtypeStruct((1, 1), F32)],
        scratch_shapes=[pltpu.VMEM((8, d), F32)], compiler_params=_params(),
    )(act, w_ffn_out, x1, target)


LATE_GATHER = ("w_o_attn", "w_pw_conv", "w_out", "w_ffn_in", "w_ffn_out")
EARLY_REDUCE = LATE_GATHER


def _blocks_by_half(g):
    if g.ndim == 2:
        g = g.reshape(N_CHIPS, g.shape[0] // N_CHIPS, g.shape[1])
    return g.reshape(N_CHIPS, 2, g.shape[1] // 2, g.shape[2])


def _forward_backward(x, pos_col, target, wts, first_gather, late_bufs, pos_arr):
    wts = dict(wts)
    consts = _rope_consts()
    bd = consts[3]
    qw2 = jnp.tile(wts["q_norm_w"], (1, LANES // HEAD_DIM))
    kw2 = jnp.tile(wts["k_norm_w"], (1, LANES // HEAD_DIM))
    qkv_w = 3 * N_SLOT_HEADS * HEAD_DIM
    conv_col0 = 3 * qkv_w

    h = _rmsnorm_fwd(x, wts["norm1_w"], "rms1_fwd")
    slot_order = jnp.bitwise_xor(pos_arr[1], jnp.asarray([0, 2, 1, 3], jnp.int32))
    blocked = lambda buf: buf.reshape(N_CHIPS, -1, buf.shape[3])
    near = first_gather
    rope = _rope_tables(pos_col, consts, "rope_tables", after=list(late_bufs))
    proj = _proj_by_slot(h, blocked(near[2][0]), slot_order, 0, 1, "mm_proj_own", after=list(rope))
    near = _split_middle(near, after=[proj], name="allgather_w_in_near_forward")
    far, _ = _split_start(_gather_both_legs_rider(near[2][:1], near[2][1:], FAR_CHIPS), "allgather_w_in_far_start")
    _, bufs = _split_wait((near[0], near[1], far[2], near[3]), after=[], name="allgather_w_in_near_wait")
    proj = _proj_by_slot(h, blocked(bufs[0]), slot_order, 1, len(NEAR_CHIPS), "mm_proj_near", proj=proj)
    far = _split_middle((far[0], far[1], bufs, far[3]), after=[proj], name="allgather_w_in_far_forward")
    (w_in_buf, conv_w_buf, b_gate_buf), _ = _split_wait(far, after=[], name="allgather_w_in_far_wait")
    wts["w_in"] = w_in_buf.reshape(N_CHIPS, -1, w_in_buf.shape[3])
    wts["conv_w"] = conv_w_buf.transpose(1, 0, 2).reshape(CONV_WIDTH, -1)
    wts["b_gate"] = b_gate_buf.transpose(1, 0, 2).reshape(2, 1, -1)
    ch = wts["conv_w"].shape[1]
    gate_col0 = conv_col0 + 2 * ch
    n_mix = LATE_GATHER.index("w_ffn_in")
    mix_rider, ffn_rider = _gather_ici_rider(late_bufs[:n_mix], []), _gather_ici_rider(late_bufs[n_mix:], [])
    both, started = _split_start(_riders_together(mix_rider, ffn_rider), "late_gather_start", after=[wts["w_in"]])
    n_sem, n_buf = len(mix_rider.scratch), len(mix_rider.operands)
    mix_gather = (mix_rider, both[1][:n_sem], both[2][:n_buf], [])
    ffn_gather = (ffn_rider, both[1][n_sem:], both[2][n_buf:], [])
    proj = _proj_by_slot(h, wts["w_in"], slot_order, 1 + len(NEAR_CHIPS), len(FAR_CHIPS), "mm_proj_far", proj=proj,
                         after=[started])
    qn, kn = _qk_fwd(proj, rope, qw2, kw2, bd, "qk_fwd")
    attn, lse, attn_b = _attn_fwd(qn, kn, proj, "attn_fwd")

    def gathered(names, bufs):
        for n, buf in zip(names, bufs):
            full = buf.reshape(N_CHIPS, -1, buf.shape[3])
            wts[n] = full.reshape(-1, full.shape[2]) if n in ROW_SHARDED else full

    mix_bufs, _ = _split_wait(mix_gather, after=[attn_b], name="late_gather_mix_wait")
    (u0, uc), mix_bufs = _conv_fwd(proj, wts["conv_w"], wts["conv_b"], conv_col0, "conv_fwd",
                                   rider=_gather_forward_rider(mix_bufs))
    gathered(LATE_GATHER[:n_mix], mix_bufs)
    u3 = _ln_silu_fwd(uc, wts["conv_ln_w"], wts["conv_ln_b"], "ln_fwd")
    ffn_bufs, _ = _split_wait(ffn_gather, after=[u3], name="late_gather_ffn_wait")
    (y_a, y_b, mixed, x1, h2), ffn_bufs = _mix_out_proj(
        attn_b, u3, wts["w_o_attn"], wts["w_pw_conv"], proj, wts["b_gate"], gate_col0, wts["w_out"], x,
        wts["norm2_w"], "mix_x1_rms2", rider=_gather_forward_rider(ffn_bufs))
    gathered(LATE_GATHER[n_mix:], ffn_bufs)
    gate, up, act = _ffn_in_swiglu(h2, wts["w_ffn_in"], "mm_gu_swiglu")
    dy, dy_b16, loss = _ffn_out_loss(act, wts["w_ffn_out"], x1, target, "mm_x2_loss")

    g = {}
    by_chip = {}

    def pair_add(n, blocks, received):
        return _add_own_half(blocks, received, pos_arr, f"grads_pair_add_{n}")

    g_ffn_out = _blocks_by_half(
        _matmul(act, dy_b16, mode="tn", tm=1408, tn=1024, tk=2048, out_dtype=F32, name="mm_dwffnout"))
    dgu, (received,) = _ffn_out_bwd_swiglu(dy_b16, wts["w_ffn_out"], gate, up, "mm_dact_swiglu_bwd",
                                           rider=_pair_exchange_rider([g_ffn_out], halved=True))
    to_send, own = pair_add("w_ffn_out", g_ffn_out, received)
    (dx1, dx1_b16, g["norm2_w"]), (by_chip["w_ffn_out"],) = _matmul_nt_rmsnorm_bwd(
        dgu, wts["w_ffn_in"], x1, wts["norm2_w"], dy, "mm_dh2_rms2_bwd", bf16_copy=True,
        rider=_chip_exchange_rider([to_send], [own]))
    g_ffn_in = _blocks_by_half(_matmul(h2, dgu, mode="tn", tm=512, tn=1408, tk=2048, out_dtype=F32,
                                       name="mm_dwffnin", out_blocked=N_CHIPS, cols_outer=True))
    exchanging, started = _split_start(_pair_exchange_rider([g_ffn_in], halved=True), "grads_ffn_in_pair_start")
    g["w_out"] = _matmul(mixed, dx1_b16, mode="tn", tm=512, tn=1024, tk=2048, out_dtype=F32, name="mm_dwout",
                         after=[started])
    dgl, dy_a, dy_b, g["b_gate"], dattn, du3 = _out_proj_bwd_gates(
        dx1_b16, wts["w_out"], proj, wts["b_gate"], y_a, y_b, wts["w_o_attn"], wts["w_pw_conv"], gate_col0,
        "mix_bwd")
    (received,), (g_ffn_in,) = _split_wait(exchanging, after=[dgl], name="grads_ffn_in_pair_wait")
    ffn_in_to_send, ffn_in_own = pair_add("w_ffn_in", g_ffn_in, received)
    g["w_o_attn"] = _matmul(attn_b, dy_a, mode="tn", tm=512, tn=256, tk=2048, out_dtype=F32, name="mm_dwo",
                            out_blocked=N_CHIPS)
    g["w_pw_conv"] = _matmul(u3, dy_b, mode="tn", tm=512, tn=256, tk=2048, out_dtype=F32, name="mm_dwpw",
                             out_blocked=N_CHIPS)
    duc, g["conv_ln_w"], g["conv_ln_b"] = _ln_silu_bwd(du3, uc, wts["conv_ln_w"], wts["conv_ln_b"], "ln_bwd")

    small3 = ("w_out", "w_o_attn", "w_pw_conv")
    g_small3 = [_blocks_by_half(g.pop(n)) for n in small3]
    (da, db, g["conv_w"], g["conv_b"]), received = _conv_bwd(
        duc, u0, proj, wts["conv_w"], conv_col0, "conv_bwd", rider=_pair_exchange_rider(g_small3, halved=True))
    sums3 = [pair_add(n, gb, rv) for n, gb, rv in zip(small3, g_small3, received)]
    (dqn, dkn, dv), (by_chip["w_ffn_in"],) = _attn_bwd(
        qn, kn, proj, dattn, attn, lse, bd, "attn_bwd",
        rider=_chip_exchange_rider([ffn_in_to_send], [ffn_in_own]))
    (dproj, dqw, dkw), exchanged3 = _qk_bwd(
        dqn, dkn, dv, da, db, dgl, proj, rope, qw2, kw2, bd, "qk_bwd",
        rider=_chip_exchange_rider([s[0] for s in sums3], [s[1] for s in sums3]))
    by_chip.update(zip(small3, exchanged3))
    halves = [_sum_chips(by_chip[n], pos_arr, f"grads_chip_sum_{n}") for n in EARLY_REDUCE]
    g["q_norm_w"] = dqw[:, :HEAD_DIM]
    g["k_norm_w"] = dkw[:, :HEAD_DIM]

    c = pos_arr[0]
    rh = h.shape[1] // 2
    h_sibling = lax.dynamic_slice_in_dim(h, (1 - c) * rh, rh, axis=1)
    h_own = lax.dynamic_slice_in_dim(h, c * rh, rh, axis=1)
    g_sibling, shards = _matmul(h_sibling, dproj, mode="tn", tm=rh, tn=1920, tk=2048, out_dtype=F32,
                                name="mm_dwin_sibling", out_blocked=N_CHIPS, rider=_pair_gather_rider(halves))
    reduced = dict(zip(EARLY_REDUCE, shards))
    exchanging, started = _split_start(_pair_exchange_rider([g_sibling], halved=False), "grads_w_in_pair_start")
    g_own = _matmul(h_own, dproj, mode="tn", tm=rh, tn=1920, tk=2048, out_dtype=F32, name="mm_dwin_own",
                    out_blocked=N_CHIPS, after=[started])
    (from_sibling,), _ = _split_wait(exchanging, after=[g_own], name="grads_w_in_pair_wait")
    to_send, own = _add_own_half(g_own, from_sibling, pos_arr, "grads_pair_add_w_in")
    in_flight, started = _split_start(_chip_exchange_rider([to_send], [own]), "grads_w_in_exchange_start")
    grad_x, g["norm1_w"] = _matmul_nt_rmsnorm_bwd(dproj, wts["w_in"], x, wts["norm1_w"], dx1, "mm_dh_rms1_bwd",
                                                  bf16_copy=False, after=[started])
    return loss, grad_x, g, reduced, (in_flight, started)


def _mesh_pos():
    return lax.axis_index("x"), lax.axis_index("y"), lax.axis_index("c")


def _other_chips(x, y):
    return [(1 - x, y), (x, 1 - y), (1 - x, 1 - y)]


NEAR_CHIPS, FAR_CHIPS = (0, 1), (2,)


def _cast_into_slot(shard, chip_arr, dtype, name, n_slots=N_CHIPS, after=()):
    r, c = shard.shape
    tr = r // 2 if r % 32 == 0 else r

    def body(chip_ref, s_ref, *refs):
        refs[-1][...] = s_ref[...].astype(dtype)

    return pl.pallas_call(
        body, name=name,
        grid_spec=pltpu.PrefetchScalarGridSpec(
            num_scalar_prefetch=1, grid=(r // tr,),
            in_specs=[pl.BlockSpec((tr, c), lambda i, chip_ref: (i, 0))] + [ANY] * len(after),
            out_specs=pl.BlockSpec((None, tr, c), lambda i, chip_ref: (chip_ref[0], i, 0))),
        out_shape=jax.ShapeDtypeStruct((n_slots, r, c), dtype), compiler_params=_params(),
    )(chip_arr, shard, *after)


GATHER_CHUNKS = 8


def _gather_both_legs_rider(big, small, peers):
    nb = len(big)
    n = nb + len(small)
    nch = GATHER_CHUNKS

    def part(bufs, a, slot, half, ch):
        if a >= nb:
            return bufs[a].at[slot]
        rows = bufs[a].shape[2] // nch
        return bufs[a].at[slot, half, pl.ds(ch * rows, rows)]

    def pieces():
        return [(a, ch, k) for ch in range(nch) for a in range(n) for k in peers if a < nb or ch == 0]

    def ici(bufs, sems, a, ch, k, slot_of_src):
        x, y, c = _mesh_pos()
        px, py = _other_chips(x, y)[k]
        slot = 2 * x + y if slot_of_src == "mine" else 2 * px + py
        return pltpu.make_async_remote_copy(
            src_ref=part(bufs, a, slot, c, ch), dst_ref=part(bufs, a, slot, c, ch), send_sem=sems[0].at[a, ch, k],
            recv_sem=sems[1].at[a, ch, k], device_id=(px, py, c), device_id_type=MESH)

    def forward(bufs, sems, a, ch, k, half):
        x, y, c = _mesh_pos()
        px, py = _other_chips(x, y)[k]
        h = c if half == "mine" else 1 - c
        return pltpu.make_async_remote_copy(
            src_ref=part(bufs, a, 2 * px + py, h, ch), dst_ref=part(bufs, a, 2 * px + py, h, ch),
            send_sem=sems[2].at[a, ch, k], recv_sem=sems[3].at[a, ch, k], device_id=(x, y, 1 - c),
            device_id_type=MESH)

    def start(r_in, bufs, sems):
        for a, ch, k in pieces():
            ici(bufs, sems, a, ch, k, "mine").start()

    def middle(r_in, bufs, sems):
        for a, ch, k in pieces():
            ici(bufs, sems, a, ch, k, "theirs").wait_recv()
            if a < nb:
                forward(bufs, sems, a, ch, k, "mine").start()
        for a, ch, k in pieces():
            ici(bufs, sems, a, ch, k, "mine").wait_send()

    def wait(r_in, bufs, sems):
        for a, ch, k in pieces():
            if a < nb:
                forward(bufs, sems, a, ch, k, "theirs").wait_recv()
        for a, ch, k in pieces():
            if a < nb:
                forward(bufs, sems, a, ch, k, "mine").wait_send()

    ops = list(big) + list(small)
    return _Rider(ops, [jax.ShapeDtypeStruct(o.shape, o.dtype) for o in ops], {i: i for i in range(n)},
                  [pltpu.SemaphoreType.DMA((n, nch, 3)), pltpu.SemaphoreType.DMA((n, nch, 3)),
                   pltpu.SemaphoreType.DMA((nb, nch, 3)), pltpu.SemaphoreType.DMA((nb, nch, 3))],
                  start, wait, middle)


def _comm_call(rider, name):
    def body():
        pass

    return _pallas(body, name=name, grid=(1,), in_specs=[], out_specs=[], out_shape=[], operands=[],
                   rider=rider)[1]


def _gather_ici_rider(big, small):
    nb = len(big)
    n = nb + len(small)

    def copies(bufs, sems):
        x, y, c = _mesh_pos()
        me = 2 * x + y
        part = lambda a, slot: bufs[a].at[slot, c] if a < nb else bufs[a].at[slot]
        out = []
        for a in range(n):
            for k, (px, py) in enumerate(_other_chips(x, y)):
                send = functools.partial(
                    pltpu.make_async_remote_copy,
                    src_ref=part(a, me), dst_ref=part(a, me), send_sem=sems[0].at[a, k],
                    recv_sem=sems[1].at[a, k], device_id=(px, py, c), device_id_type=MESH)
                recv = functools.partial(
                    pltpu.make_async_remote_copy,
                    src_ref=part(a, 2 * px + py), dst_ref=part(a, 2 * px + py), send_sem=sems[0].at[a, k],
                    recv_sem=sems[1].at[a, k], device_id=(px, py, c), device_id_type=MESH)
                out.append((send, recv))
        return out

    def start(r_in, r_out, sems):
        for send, _ in copies(r_out, sems):
            send().start()

    def wait(r_in, r_out, sems):
        cps = copies(r_out, sems)
        for _, recv in cps:
            recv().wait_recv()
        for send, _ in cps:
            send().wait_send()

    ops = list(big) + list(small)
    return _Rider(ops, [jax.ShapeDtypeStruct(o.shape, o.dtype) for o in ops], {i: i for i in range(n)},
                  [pltpu.SemaphoreType.DMA((n, 3)), pltpu.SemaphoreType.DMA((n, 3))], start, wait)


def _gather_forward_rider(big):
    n = len(big)

    def copies(bufs, sems):
        x, y, c = _mesh_pos()
        out = []
        for a in range(n):
            for k, (px, py) in enumerate(_other_chips(x, y)):
                slot = 2 * px + py
                send = functools.partial(
                    pltpu.make_async_remote_copy,
                    src_ref=bufs[a].at[slot, c], dst_ref=bufs[a].at[slot, c], send_sem=sems[0].at[a, k],
                    recv_sem=sems[1].at[a, k], device_id=(x, y, 1 - c), device_id_type=MESH)
                recv = functools.partial(
                    pltpu.make_async_remote_copy,
                    src_ref=bufs[a].at[slot, 1 - c], dst_ref=bufs[a].at[slot, 1 - c], send_sem=sems[0].at[a, k],
                    recv_sem=sems[1].at[a, k], device_id=(x, y, 1 - c), device_id_type=MESH)
                out.append((send, recv))
        return out

    def start(r_in, r_out, sems):
        for send, _ in copies(r_out, sems):
            send().start()

    def wait(r_in, r_out, sems):
        cps = copies(r_out, sems)
        for _, recv in cps:
            recv().wait_recv()
        for send, _ in cps:
            send().wait_send()

    return _Rider(big, [jax.ShapeDtypeStruct(o.shape, o.dtype) for o in big], {i: i for i in range(n)},
                  [pltpu.SemaphoreType.DMA((n, 3)), pltpu.SemaphoreType.DMA((n, 3))], start, wait)


def _pair_exchange_rider(gs, halved):
    n = len(gs)

    def copies(r_in, r_out, sems):
        x, y, c = _mesh_pos()
        return [pltpu.make_async_remote_copy(
            src_ref=r_in[a].at[:, 1 - c] if halved else r_in[a], dst_ref=r_out[a], send_sem=sems[0].at[a],
            recv_sem=sems[1].at[a], device_id=(x, y, 1 - c), device_id_type=MESH) for a in range(n)]

    def start(r_in, r_out, sems):
        for cp in copies(r_in, r_out, sems):
            cp.start()

    def wait(r_in, r_out, sems):
        for cp in copies(r_in, r_out, sems):
            cp.wait()

    return _Rider(gs, [jax.ShapeDtypeStruct((g.shape[0],) + g.shape[-2:], g.dtype) for g in gs], {},
                  [pltpu.SemaphoreType.DMA((n,)), pltpu.SemaphoreType.DMA((n,))], start, wait)


def _chip_exchange_rider(to_send, by_chip, row_range=None):
    n = len(to_send)

    def copies(r_in, r_out, sems):
        x, y, c = _mesh_pos()
        me = 2 * x + y
        rows = (lambda ref: ref) if row_range is None else (lambda ref: ref.at[pl.ds(*row_range)])
        out = []
        for a in range(n):
            for k, (px, py) in enumerate(_other_chips(x, y)):
                send = functools.partial(
                    pltpu.make_async_remote_copy,
                    src_ref=rows(r_in[a].at[2 * px + py]), dst_ref=rows(r_out[a].at[me]),
                    send_sem=sems[0].at[a, k], recv_sem=sems[1].at[a, k], device_id=(px, py, c),
                    device_id_type=MESH)
                recv = functools.partial(
                    pltpu.make_async_remote_copy,
                    src_ref=rows(r_in[a].at[me]), dst_ref=rows(r_out[a].at[2 * px + py]),
                    send_sem=sems[0].at[a, k], recv_sem=sems[1].at[a, k], device_id=(px, py, c),
                    device_id_type=MESH)
                out.append((send, recv))
        return out

    def start(r_in, r_out, sems):
        for send, _ in copies(r_in, r_out, sems):
            send().start()

    def wait(r_in, r_out, sems):
        cps = copies(r_in, r_out, sems)
        for _, recv in cps:
            recv().wait_recv()
        for send, _ in cps:
            send().wait_send()

    return _Rider(list(to_send) + list(by_chip), [jax.ShapeDtypeStruct(b.shape, b.dtype) for b in by_chip],
                  {n + i: i for i in range(n)},
                  [pltpu.SemaphoreType.DMA((n, 3)), pltpu.SemaphoreType.DMA((n, 3))], start, wait)


HBM = pl.BlockSpec(memory_space=pltpu.HBM)
SEM = pl.BlockSpec(memory_space=pltpu.SEMAPHORE)


_IN_FLIGHT = pltpu.CompilerParams(has_side_effects=pltpu.SideEffectType.DATAFLOW_SIDE_EFFECTING)


class _FlatSems:
    def __init__(self, ref, shape):
        self.ref, self.shape = ref, shape

    @property
    def at(self):
        return self

    def __getitem__(self, idx):
        idx = idx if isinstance(idx, tuple) else (idx,)
        flat = 0
        for i, n in zip(idx, self.shape):
            flat = flat * n + i
        return self.ref.at[flat]


def _flat_sem_types(rider):
    return tuple(pltpu.SemaphoreType.DMA((int(np.prod(s.shape)),)) for s in rider.scratch)


def _as_rider_sems(rider, refs):
    return [_FlatSems(r, s.shape) for r, s in zip(refs, rider.scratch)]


def _split_start(rider, name, after=()):
    n_in, n_out, n_sem = len(rider.operands), len(rider.out_shapes), len(rider.scratch)
    n_after = len(after)
    fresh = [j for j in range(n_out) if j not in rider.aliases.values()]
    by_out = {j: i for i, j in rider.aliases.items()}

    def body(*refs):
        r_in = refs[:n_in]
        refs = refs[n_in + n_after:]
        sems = refs[:n_sem]
        thru = refs[n_sem:n_sem + n_in]
        fresh_refs = refs[n_sem + n_in:n_sem + n_in + len(fresh)]
        token = refs[-1]
        r_out = [thru[by_out[j]] if j in by_out else fresh_refs[fresh.index(j)] for j in range(n_out)]
        rider.start(r_in, r_out, _as_rider_sems(rider, sems))
        token[...] = jnp.zeros_like(token)

    res = pl.pallas_call(
        body, name=name,
        out_shape=_flat_sem_types(rider) + tuple(pltpu.HBM(o.shape, o.dtype) for o in rider.operands)
        + tuple(pltpu.HBM(rider.out_shapes[j].shape, rider.out_shapes[j].dtype) for j in fresh)
        + (jax.ShapeDtypeStruct((8, LANES), F32),),
        in_specs=(HBM,) * n_in + (ANY,) * n_after,
        out_specs=(SEM,) * n_sem + (HBM,) * (n_in + len(fresh)) + (pl.BlockSpec(memory_space=pltpu.VMEM),),
        input_output_aliases={i: n_sem + i for i in range(n_in)}, compiler_params=_IN_FLIGHT,
    )(*[pltpu.with_memory_space_constraint(o, pltpu.HBM) for o in rider.operands], *after)
    return (rider, res[:n_sem], res[n_sem:n_sem + n_in], res[n_sem + n_in:-1]), res[-1]


def _split_continue(handles, after, name, phase):
    rider, sems, thru, fresh_arrays = handles
    n_in, n_out, n_sem = len(rider.operands), len(rider.out_shapes), len(rider.scratch)
    fresh = [j for j in range(n_out) if j not in rider.aliases.values()]
    by_out = {j: i for i, j in rider.aliases.items()}
    n_data = n_in + len(fresh)

    def body(*refs):
        r_in = refs[:n_in]
        fresh_refs = refs[n_in:n_data]
        sem_refs = refs[n_data:n_data + n_sem]
        r_out = [r_in[by_out[j]] if j in by_out else fresh_refs[fresh.index(j)] for j in range(n_out)]
        phase(r_in, r_out, _as_rider_sems(rider, sem_refs))

    data = list(thru) + list(fresh_arrays)
    return pl.pallas_call(
        body, name=name, out_shape=tuple(pltpu.HBM(d.shape, d.dtype) for d in data),
        in_specs=(HBM,) * n_data + (SEM,) * n_sem + (ANY,) * len(after), out_specs=(HBM,) * n_data,
        input_output_aliases={i: i for i in range(n_data)}, compiler_params=_IN_FLIGHT,
    )(*data, *sems, *after)


def _split_middle(handles, after, name):
    rider, sems, thru, _ = handles
    res = _split_continue(handles, after, name, rider.middle)
    return rider, sems, res[:len(thru)], res[len(thru):]


def _split_wait(handles, after, name):
    rider = handles[0]
    n_in, n_out = len(rider.operands), len(rider.out_shapes)
    fresh = [j for j in range(n_out) if j not in rider.aliases.values()]
    by_out = {j: i for i, j in rider.aliases.items()}
    res = _split_continue(handles, after, name, rider.wait)
    return [res[by_out[j]] if j in by_out else res[n_in + fresh.index(j)] for j in range(n_out)], res[:n_in]


def _pair_gather_rider(bufs):
    n = len(bufs)

    def copies(r_out, sems):
        x, y, c = _mesh_pos()
        out = []
        for a in range(n):
            send = functools.partial(
                    pltpu.make_async_remote_copy,
                src_ref=r_out[a].at[c], dst_ref=r_out[a].at[c], send_sem=sems[0].at[a],
                recv_sem=sems[1].at[a], device_id=(x, y, 1 - c), device_id_type=MESH)
            recv = functools.partial(
                    pltpu.make_async_remote_copy,
                src_ref=r_out[a].at[1 - c], dst_ref=r_out[a].at[1 - c], send_sem=sems[0].at[a],
                recv_sem=sems[1].at[a], device_id=(x, y, 1 - c), device_id_type=MESH)
            out.append((send, recv))
        return out

    def start(r_in, r_out, sems):
        for send, _ in copies(r_out, sems):
            send().start()

    def wait(r_in, r_out, sems):
        cps = copies(r_out, sems)
        for _, recv in cps:
            recv().wait_recv()
        for send, _ in cps:
            send().wait_send()

    return _Rider(bufs, [jax.ShapeDtypeStruct(b.shape, b.dtype) for b in bufs], {i: i for i in range(n)},
                  [pltpu.SemaphoreType.DMA((n,)), pltpu.SemaphoreType.DMA((n,))], start, wait)


def _add_own_half(g, recv, pos_arr, name):
    nb, rh, cols = g.shape[0], g.shape[-2], g.shape[-1]

    def body(pos_ref, g_ref, r_ref, send_ref, own_ref):
        s = (g_ref[...] + r_ref[...]).astype(BF16)
        send_ref[...] = s

        @pl.when(pl.program_id(0) == pos_ref[1])
        def _():
            own_ref[...] = s

    blk = pl.BlockSpec((None, rh, cols), lambda j, pos_ref: (j, 0, 0))
    g_spec = blk if g.ndim == 3 else pl.BlockSpec((None, None, rh, cols),
                                                   lambda j, pos_ref: (j, pos_ref[0], 0, 0))
    shape = jax.ShapeDtypeStruct((nb, rh, cols), BF16)
    return pl.pallas_call(
        body, name=name,
        grid_spec=pltpu.PrefetchScalarGridSpec(
            num_scalar_prefetch=1, grid=(nb,), in_specs=[g_spec, blk],
            out_specs=[blk, pl.BlockSpec((None, rh, cols), lambda j, pos_ref: (pos_ref[1], 0, 0))]),
        out_shape=[shape, shape], compiler_params=_params(),
    )(pos_arr, g, recv)


def _sum_chips(gath, pos_arr, name):
    nb, rh, cols = gath.shape

    def body(pos_ref, a_ref, b_ref, c_ref, d_ref, o_ref):
        del pos_ref
        o_ref[...] = ((a_ref[...].astype(F32) + b_ref[...].astype(F32)) + c_ref[...].astype(F32)) \
            + d_ref[...].astype(F32)

    tr = rh // 2 if (rh // 2) % 16 == 0 else rh
    specs = [pl.BlockSpec((None, tr, cols), functools.partial(lambda i, pos_ref, j: (j, i, 0), j=j))
             for j in range(nb)]
    return pl.pallas_call(
        body, name=name,
        grid_spec=pltpu.PrefetchScalarGridSpec(
            num_scalar_prefetch=1, grid=(rh // tr,), in_specs=specs,
            out_specs=pl.BlockSpec((None, tr, cols), lambda i, pos_ref: (pos_ref[0], i, 0))),
        out_shape=jax.ShapeDtypeStruct((2, rh, cols), F32), compiler_params=_params(),
    )(pos_arr, gath, gath, gath, gath)


N_DEVICES = 8


def _small_gather_rider(buf):
    def copies(r_out, sems):
        x, y, c = _mesh_pos()
        me = 4 * x + 2 * y + c
        out = []
        for r in range(1, N_DEVICES):
            px = 1 - x if r & 4 else x
            py = 1 - y if r & 2 else y
            pc = 1 - c if r & 1 else c
            out.append(pltpu.make_async_remote_copy(
                src_ref=r_out[0].at[me], dst_ref=r_out[0].at[me], send_sem=sems[0].at[r - 1],
                recv_sem=sems[1].at[r - 1], device_id=(px, py, pc), device_id_type=MESH))
        return out

    def start(r_in, r_out, sems):
        for cp in copies(r_out, sems):
            cp.start()

    def wait(r_in, r_out, sems):
        cps = copies(r_out, sems)
        for cp in cps:
            cp.wait_recv()
        for cp in cps:
            cp.wait_send()

    return _Rider([buf], [jax.ShapeDtypeStruct(buf.shape, buf.dtype)], {0: 0},
                  [pltpu.SemaphoreType.DMA((N_DEVICES - 1,)), pltpu.SemaphoreType.DMA((N_DEVICES - 1,))],
                  start, wait)


def _sum_devices(buf, name):
    def body(b_ref, o_ref):
        acc = b_ref[0]
        for i in range(1, N_DEVICES):
            acc = acc + b_ref[i]
        o_ref[...] = acc

    return _pallas(body, name=name, grid=(1,), in_specs=[pl.BlockSpec(buf.shape, lambda i: (0, 0, 0))],
                   out_specs=pl.BlockSpec(buf.shape[1:], lambda i: (0, 0)),
                   out_shape=jax.ShapeDtypeStruct(buf.shape[1:], F32), operands=[buf])


def _adamw_math(w, g, m, v):
    m = ADAM_B1 * m + (1.0 - ADAM_B1) * g
    v = ADAM_B2 * v + (1.0 - ADAM_B2) * (g * g)
    m_hat = m / (1.0 - ADAM_B1 ** ADAM_STEP)
    v_hat = v / (1.0 - ADAM_B2 ** ADAM_STEP)
    delta = -ADAM_LR * (m_hat / (jnp.sqrt(v_hat) + ADAM_EPS) + ADAM_WD * w)
    return delta, m, v


def _adamw(w, g, m, v, name, after=()):
    r, c = w.shape
    tr = next(t for t in (256, 352, 128, 64) if r % t == 0 and r >= 2 * t)

    def body(w_ref, g_ref, m_ref, v_ref, go_ref, d_ref, mo_ref, vo_ref):
        gv = g_ref[...]
        d, mn, vn = _adamw_math(w_ref[...], gv, m_ref[...], v_ref[...])
        go_ref[...] = gv
        d_ref[...] = d
        mo_ref[...] = mn
        vo_ref[...] = vn

    blk = pl.BlockSpec((tr, c), lambda i: (i, 0))
    return _pallas(body, name=name, grid=(r // tr,), in_specs=[blk] * 4, out_specs=[blk] * 4,
                   out_shape=[jax.ShapeDtypeStruct((r, c), F32)] * 4, operands=[w, g, m, v], after=after)


def _adamw_small(ws, gs, ms, vs, name):
    n = len(ws)

    def body(*refs):
        w_r, g_r, m_r, v_r = refs[:n], refs[n:2 * n], refs[2 * n:3 * n], refs[3 * n:4 * n]
        d_o, m_o, v_o = refs[4 * n:5 * n], refs[5 * n:6 * n], refs[6 * n:7 * n]
        for i in range(n):
            d, mn, vn = _adamw_math(w_r[i][...], g_r[i][...], m_r[i][...], v_r[i][...])
            d_o[i][...] = d
            m_o[i][...] = mn
            v_o[i][...] = vn

    specs = [pl.BlockSpec(w.shape, lambda i: (0, 0)) for w in ws]
    shapes = [jax.ShapeDtypeStruct(w.shape, F32) for w in ws]
    outs = pl.pallas_call(
        body, name=name, grid=(1,), in_specs=specs * 4, out_specs=specs * 3, out_shape=shapes * 3,
        compiler_params=_params(),
    )(*ws, *gs, *ms, *vs)
    return outs[:n], outs[n:2 * n], outs[2 * n:]


ROW_SHARDED = ("w_out", "w_ffn_out")
SMALL = ("norm1_w", "b_gate", "q_norm_w", "k_norm_w", "conv_w", "conv_b", "conv_ln_w", "conv_ln_b", "norm2_w")
ORDER = ("norm1_w", "w_in", "b_gate", "q_norm_w", "k_norm_w", "w_o_attn", "conv_w", "conv_b", "conv_ln_w",
         "conv_ln_b", "w_pw_conv", "w_out", "norm2_w", "w_ffn_in", "w_ffn_out")
PACK_TILE = 8 * LANES


def _pack_small(parts):
    rows = []
    for p in parts:
        flat = p.reshape(-1)
        pad = (-flat.shape[0]) % PACK_TILE
        rows.append(jnp.pad(flat, (0, pad)).reshape(-1, LANES))
    return jnp.concatenate(rows, axis=0)


def _unpack_small(packed, shapes):
    out, row = [], 0
    for shp in shapes:
        size = int(np.prod(shp))
        nrow = -(-size // PACK_TILE) * (PACK_TILE // LANES)
        out.append(packed[row:row + nrow].reshape(-1)[:size].reshape(shp))
        row += nrow
    return out


def kernel(x, positions, norm1_w, w_in, b_gate, q_norm_w, k_norm_w, w_o_attn, conv_w, conv_b, conv_ln_w, conv_ln_b, w_pw_conv, w_out, norm2_w, w_ffn_in, w_ffn_out, loss_target, m_norm1_w, m_w_in, m_b_gate, m_q_norm_w, m_k_norm_w, m_w_o_attn, m_conv_w, m_conv_b, m_conv_ln_w, m_conv_ln_b, m_w_pw_conv, m_w_out, m_norm2_w, m_w_ffn_in, m_w_ffn_out, v_norm1_w, v_w_in, v_b_gate, v_q_norm_w, v_k_norm_w, v_w_o_attn, v_conv_w, v_conv_b, v_conv_ln_w, v_conv_ln_b, v_w_pw_conv, v_w_out, v_norm2_w, v_w_ffn_in, v_w_ffn_out):
    w = dict(norm1_w=norm1_w, w_in=w_in, b_gate=b_gate, q_norm_w=q_norm_w, k_norm_w=k_norm_w, w_o_attn=w_o_attn,
             conv_w=conv_w, conv_b=conv_b, conv_ln_w=conv_ln_w, conv_ln_b=conv_ln_b, w_pw_conv=w_pw_conv,
             w_out=w_out, norm2_w=norm2_w, w_ffn_in=w_ffn_in, w_ffn_out=w_ffn_out)
    m = dict(norm1_w=m_norm1_w, w_in=m_w_in, b_gate=m_b_gate, q_norm_w=m_q_norm_w, k_norm_w=m_k_norm_w,
             w_o_attn=m_w_o_attn, conv_w=m_conv_w, conv_b=m_conv_b, conv_ln_w=m_conv_ln_w,
             conv_ln_b=m_conv_ln_b, w_pw_conv=m_w_pw_conv, w_out=m_w_out, norm2_w=m_norm2_w,
             w_ffn_in=m_w_ffn_in, w_ffn_out=m_w_ffn_out)
    v = dict(norm1_w=v_norm1_w, w_in=v_w_in, b_gate=v_b_gate, q_norm_w=v_q_norm_w, k_norm_w=v_k_norm_w,
             w_o_attn=v_w_o_attn, conv_w=v_conv_w, conv_b=v_conv_b, conv_ln_w=v_conv_ln_w,
             conv_ln_b=v_conv_ln_b, w_pw_conv=v_w_pw_conv, w_out=v_w_out, norm2_w=v_norm2_w,
             w_ffn_in=v_w_ffn_in, w_ffn_out=v_w_ffn_out)
    cx, cy, cc = _mesh_pos()
    chip = 2 * cx + cy

    chip_arr = chip.reshape(1).astype(jnp.int32)
    pos_arr = jnp.stack([cc, chip]).astype(jnp.int32)
    halves = lambda buf: buf.reshape(N_CHIPS, 2, buf.shape[1] // 2, buf.shape[2])
    w_in_buf = halves(_cast_into_slot(w["w_in"][0], chip_arr, BF16, "cast_w_in"))
    small_bufs = [_cast_into_slot(w[n][0], chip_arr, F32, f"slot_{n}") for n in ("conv_w", "b_gate")]
    first_gather, started = _split_start(_gather_both_legs_rider([w_in_buf], small_bufs, NEAR_CHIPS),
                                         "allgather_w_in_near_start")
    late_bufs = [halves(_cast_into_slot(w[n][0], chip_arr, BF16, f"cast_{n}", after=[started]))
                 for n in LATE_GATHER]
    wts = dict(norm1_w=norm1_w, q_norm_w=q_norm_w, k_norm_w=k_norm_w, conv_b=conv_b, conv_ln_w=conv_ln_w,
               conv_ln_b=conv_ln_b, norm2_w=norm2_w)

    loss, grad_x, g, reduced, w_in_in_flight = _forward_backward(
        x[0], positions.reshape(-1, 1), loss_target[0], wts, first_gather, late_bufs, pos_arr)
    grads = {n: b.reshape(-1, b.shape[2]) for n, b in reduced.items()}

    w_in_in_flight, started = w_in_in_flight
    delta, new_m, new_v = {}, {}, {}
    for n in EARLY_REDUCE:
        grads[n], delta[n], new_m[n], new_v[n] = _adamw(w[n][0], grads[n], m[n][0], v[n][0], f"adamw_{n}",
                                                        after=[started])
    small_parts = [loss] + [g[n] for n in SMALL]
    small_shapes = [p.shape for p in small_parts]
    device_arr = (4 * cx + 2 * cy + cc).reshape(1).astype(jnp.int32)
    small_buf = _cast_into_slot(_pack_small(small_parts), device_arr, F32, "slot_small", n_slots=N_DEVICES)

    (by_chip_w_in,), _ = _split_wait(w_in_in_flight, after=[delta[n] for n in EARLY_REDUCE] + [small_buf],
                                     name="grads_w_in_exchange_wait")
    half_w_in = _sum_chips(by_chip_w_in, pos_arr, "grads_chip_sum_w_in")
    shard_w_in, small_buf = _comm_call(
        _riders_together(_pair_gather_rider([half_w_in]), _small_gather_rider(small_buf)),
        "grads_pair_gather_w_in_small_gather")
    summed = _sum_devices(small_buf, "small_sum")
    reduced = _unpack_small(summed, small_shapes)
    loss_total = reduced[0].reshape(())
    for n, r in zip(SMALL, reduced[1:]):
        grads[n] = r
    ch_shard = conv_w.shape[2]
    grads["conv_w"] = lax.dynamic_slice_in_dim(grads["conv_w"], chip * ch_shard, ch_shard, axis=1)
    d_shard = b_gate.shape[2]
    grads["b_gate"] = lax.dynamic_slice_in_dim(grads["b_gate"], chip * d_shard, d_shard, axis=1)

    grads["w_in"], delta["w_in"], new_m["w_in"], new_v["w_in"] = _adamw(
        w["w_in"][0], shard_w_in.reshape(-1, shard_w_in.shape[2]), m["w_in"][0], v["w_in"][0], "adamw_w_in")
    flat2 = lambda a: a.reshape(-1, a.shape[-1])
    d_s, m_s, v_s = _adamw_small([flat2(w[n]) for n in SMALL], [flat2(grads[n]) for n in SMALL],
                                 [flat2(m[n]) for n in SMALL], [flat2(v[n]) for n in SMALL], "adamw_small")
    for i, n in enumerate(SMALL):
        delta[n], new_m[n], new_v[n] = d_s[i], m_s[i], v_s[i]

    shaped = lambda d, n: d[n].reshape(w[n].shape)
    return (loss_total, grad_x[None], *[shaped(grads, n) for n in ORDER], *[shaped(delta, n) for n in ORDER],
            *[shaped(new_m, n) for n in ORDER], *[shaped(new_v, n) for n in ORDER])
```
